```python
import jax
import jax.numpy as jnp
from jax import lax
import numpy as np

D_MODEL = 1024
BATCH = 8
SEQ = 2048
DEPTH = 1

GRID_W = 64
CTX_LEN = 256
EPS = 1e-6
HG_HEADS = 4
HG_DIM = 128
HG_WIDTH = HG_HEADS * HG_DIM
CHUNK = 32
ATT_HEADS = 8
ATT_KV_HEADS = 2
HEAD_DIM = 64
ATT_WIDTH = ATT_HEADS * HEAD_DIM
KV_WIDTH = ATT_KV_HEADS * HEAD_DIM
WINDOW = 128
BLOCK = 128
ROPE_THETA = 10000.0
D_FF = ((8 * D_MODEL + 3 * 256 - 1) // (3 * 256)) * 256
CTX_COLS = 3 * HG_WIDTH + 2 * KV_WIDTH
IN_COLS = CTX_COLS + 2 * HG_WIDTH + ATT_WIDTH + 2 * D_MODEL

kernel_name = 'hybrid_hgrn2_swa_prefix_dit_block'


def _rmsnorm(x, w):
    xf = x.astype(jnp.float32)
    y = xf * lax.rsqrt(jnp.mean(xf * xf, axis=-1, keepdims=True) + EPS)
    return (y * w.astype(jnp.float32)).astype(x.dtype)


def _modulate(x, w, shift, scale):
    return _rmsnorm(x, w) * (1.0 + scale) + shift


def _heads(t, n_heads, head_dim):
    return t.reshape(t.shape[0], t.shape[1], n_heads, head_dim)


def _split_ctx_side(p):
    a, b, c_, d = HG_WIDTH, 2 * HG_WIDTH, 3 * HG_WIDTH, 3 * HG_WIDTH + KV_WIDTH
    return p[..., :a], p[..., a:b], p[..., b:c_], p[..., c_:d], p[..., d:CTX_COLS]


def _split_query_side(p):
    a = CTX_COLS + HG_WIDTH
    b = a + HG_WIDTH
    c_ = b + ATT_WIDTH
    return p[..., CTX_COLS:a], p[..., a:b], p[..., b:c_], p[..., c_:IN_COLS]


def _gla_chunkwise(q, k, v, log_f, s0):
    b_, s_, h_, dk = k.shape
    dv = v.shape[-1]
    n = s_ // CHUNK

    def chunks(t):
        return t.astype(jnp.float32).reshape(b_, n, CHUNK, h_, t.shape[-1]).transpose(0, 3, 1, 2, 4)

    k, v, log_f = chunks(k), chunks(v), chunks(log_f)
    cum = jnp.cumsum(log_f, axis=3)
    cum_last = cum[:, :, :, -1:, :]
    u = jnp.einsum('bhnck,bhncv->bhnkv', k * jnp.exp(cum_last - cum), v)
    decay = jnp.exp(cum_last[:, :, :, 0, :])

    def step(state, inp):
        d, un = inp
        return d[..., None] * state + un, state

    s_final, s_start = lax.scan(step, s0.astype(jnp.float32),
                                (jnp.moveaxis(decay, 2, 0), jnp.moveaxis(u, 2, 0)))
    if q is None:
        return None, s_final
    s_start = jnp.moveaxis(s_start, 0, 2)
    qd = chunks(q) * jnp.exp(cum)
    scores = jnp.einsum('bhnck,bhnsk->bhncs', qd, k * jnp.exp(-cum))
    lower_tri = jnp.tril(jnp.ones((CHUNK, CHUNK), dtype=bool))
    scores = jnp.where(lower_tri, scores, 0.0)
    o = (jnp.einsum('bhncs,bhnsv->bhncv', scores, v)
         + jnp.einsum('bhnck,bhnkv->bhncv', qd, s_start))
    o = o.transpose(0, 2, 3, 1, 4).reshape(b_, s_, h_, dv)
    return o, s_final


def _hgrn_query(q_raw):
    return _heads(jax.nn.silu(q_raw.astype(jnp.float32)) * HG_DIM ** -0.5, HG_HEADS, HG_DIM)


def _hgrn_direction(q, f_logit, inp, lb, s0, reverse):
    f = lb + (1.0 - lb) * jax.nn.sigmoid(f_logit.astype(jnp.float32))
    log_f = _heads(jnp.log(f), HG_HEADS, HG_DIM)
    k = _heads(1.0 - f, HG_HEADS, HG_DIM)
    v = _heads(inp, HG_HEADS, HG_DIM)
    if reverse:
        q = None if q is None else jnp.flip(q, 1)
        k, v, log_f = jnp.flip(k, 1), jnp.flip(v, 1), jnp.flip(log_f, 1)
    o, s = _gla_chunkwise(q, k, v, log_f, s0)
    if reverse and o is not None:
        o = jnp.flip(o, 1)
    return o, s


def _hgrn_readout(o, g_raw, norm_w, dtype):
    g = _heads(g_raw, HG_HEADS, HG_DIM).astype(jnp.float32)
    y = _rmsnorm(o, norm_w) * jax.nn.silu(g)
    return y.reshape(o.shape[0], o.shape[1], HG_WIDTH).astype(dtype)


def _rope_1d(t, pos):
    d = t.shape[-1]
    inv_freq = ROPE_THETA ** (-jnp.arange(0, d, 2, dtype=jnp.float32) / d)
    ang = pos.astype(jnp.float32)[:, None] * inv_freq[None, :]
    cos = jnp.cos(ang)[None, :, None, :]
    sin = jnp.sin(ang)[None, :, None, :]
    tf = t.astype(jnp.float32)
    t1, t2 = tf[..., : d // 2], tf[..., d // 2:]
    return jnp.concatenate([t1 * cos - t2 * sin, t1 * sin + t2 * cos], axis=-1)


def _axial_rope(t, rows, cols):
    half = t.shape[-1] // 2
    return jnp.concatenate([_rope_1d(t[..., :half], rows), _rope_1d(t[..., half:], cols)],
                           axis=-1).astype(t.dtype)


def _window_attention(q, k, v, kc, vc, sinks):
    b_, s_, hq, dh = q.shape
    hkv = k.shape[2]
    grp = hq // hkv
    nb = s_ // BLOCK
    f32 = jnp.float32
    qb = q.astype(f32).reshape(b_, nb, BLOCK, hkv, grp, dh) * dh ** -0.5

    def band(t):
        tp = jnp.pad(t.astype(f32), ((0, 0), (BLOCK, BLOCK), (0, 0), (0, 0)))
        tp = tp.reshape(b_, nb + 2, BLOCK, hkv, dh)
        return jnp.concatenate([tp[:, :-2], tp[:, 1:-1], tp[:, 2:]], axis=2)

    kw, vw = band(k), band(v)
    s_loc = jnp.einsum('bnqhgd,bnkhd->bnhgqk', qb, kw)
    qi = jnp.arange(BLOCK)[:, None]
    kj = jnp.arange(3 * BLOCK)[None, :]
    k_pos = (jnp.arange(nb)[:, None, None] - 1) * BLOCK + kj[None]
    valid = (jnp.abs(kj - BLOCK - qi) <= WINDOW)[None] & (k_pos >= 0) & (k_pos < s_)
    s_loc = jnp.where(valid[None, :, None, None], s_loc, -jnp.inf)
    s_ctx = jnp.einsum('bnqhgd,blhd->bnhgql', qb, kc.astype(f32))
    sink = jnp.broadcast_to(sinks.astype(f32).reshape(1, 1, hkv, grp, 1, 1), s_loc.shape[:-1] + (1,))
    p = jax.nn.softmax(jnp.concatenate([s_loc, s_ctx, sink], axis=-1), axis=-1)
    n_loc = 3 * BLOCK
    n_ctx = kc.shape[1]
    o = (jnp.einsum('bnhgqk,bnkhd->bnqhgd', p[..., :n_loc], vw)
         + jnp.einsum('bnhgql,blhd->bnqhgd', p[..., n_loc:n_loc + n_ctx], vc.astype(f32)))
    return o.reshape(b_, s_, hq * dh).astype(v.dtype)


def _context_attention(qc, kc, vc, sinks):
    b_, l_, hq, dh = qc.shape
    hkv = kc.shape[2]
    grp = hq // hkv
    f32 = jnp.float32
    qg = qc.astype(f32).reshape(b_, l_, hkv, grp, dh) * dh ** -0.5
    s = jnp.einsum('blhgd,bmhd->bhglm', qg, kc.astype(f32))
    sink = jnp.broadcast_to(sinks.astype(f32).reshape(1, hkv, grp, 1, 1), s.shape[:-1] + (1,))
    p = jax.nn.softmax(jnp.concatenate([s, sink], axis=-1), axis=-1)
    o = jnp.einsum('bhglm,bmhd->blhgd', p[..., :l_], vc.astype(f32))
    return o.reshape(b_, l_, hq * dh).astype(vc.dtype)


def _merge(y_hg, y_at, gates, w_bh, w_ba, w_o):
    g_hg, g_at = jnp.split(gates, 2, axis=-1)
    mixed = jax.nn.sigmoid(g_hg) * (y_hg @ w_bh) + jax.nn.sigmoid(g_at) * (y_at @ w_ba)
    return mixed @ w_o


def _swiglu(h, w_gate, w_up, w_down):
    return (jax.nn.silu(h @ w_gate) * (h @ w_up)) @ w_down


def _fwd_setup_inputs(seed: int = 0) -> dict:
    key = jax.random.key(seed)
    ks = jax.random.split(key, 20)
    f32 = jnp.float32

    def nrm(k, shape, scale):
        return jax.random.normal(k, shape, f32) * scale

    return {
        'x': nrm(ks[0], (BATCH, SEQ, D_MODEL), 1.0),
        'c': nrm(ks[1], (BATCH, D_MODEL), 1.0),
        'ctx': nrm(ks[2], (BATCH, CTX_LEN, D_MODEL), 1.0),
        'c_ctx': nrm(ks[3], (D_MODEL,), 1.0),
        'w_ada': nrm(ks[4], (DEPTH, D_MODEL, 6 * D_MODEL), 0.5 * D_MODEL ** -0.5),
        'b_ada': nrm(ks[5], (DEPTH, 6 * D_MODEL), 0.02),
        'norm_mix_w': 1.0 + nrm(ks[6], (DEPTH, D_MODEL), 0.02),
        'norm_ffn_w': 1.0 + nrm(ks[7], (DEPTH, D_MODEL), 0.02),
        'w_in': nrm(ks[8], (DEPTH, D_MODEL, IN_COLS), D_MODEL ** -0.5),
        'hgrn_lb_logits': nrm(ks[9], (2, DEPTH + 1, HG_WIDTH), 0.5),
        'hgrn_norm_w': 1.0 + nrm(ks[10], (DEPTH, HG_DIM), 0.02),
        'q_norm_w': 1.0 + nrm(ks[11], (DEPTH, HEAD_DIM), 0.02),
        'k_norm_w': 1.0 + nrm(ks[12], (DEPTH, HEAD_DIM), 0.02),
        'attn_sinks': nrm(ks[13], (DEPTH, ATT_HEADS), 0.5),
        'w_branch_hgrn': nrm(ks[14], (DEPTH, HG_WIDTH, D_MODEL), HG_WIDTH ** -0.5),
        'w_branch_attn': nrm(ks[15], (DEPTH, ATT_WIDTH, D_MODEL), ATT_WIDTH ** -0.5),
        'w_out': nrm(ks[16], (DEPTH, D_MODEL, D_MODEL), D_MODEL ** -0.5),
        'w_ffn_gate': nrm(ks[17], (DEPTH, D_MODEL, D_FF), D_MODEL ** -0.5),
        'w_ffn_up': nrm(ks[18], (DEPTH, D_MODEL, D_FF), D_MODEL ** -0.5),
        'w_ffn_down': nrm(ks[19], (DEPTH, D_FF, D_MODEL), D_FF ** -0.5),
    }


def _fwd_reference(x, c, ctx, c_ctx, w_ada, b_ada, norm_mix_w, norm_ffn_w, w_in, hgrn_lb_logits,
              hgrn_norm_w, q_norm_w, k_norm_w, attn_sinks, w_branch_hgrn, w_branch_attn,
              w_out, w_ffn_gate, w_ffn_up, w_ffn_down):
    n_lat = x.shape[1]
    ROWS = n_lat // GRID_W
    rows = jnp.repeat(jnp.arange(ROWS), GRID_W)
    cols = jnp.tile(jnp.arange(GRID_W), ROWS)
    lower_bounds = jnp.cumsum(jax.nn.softmax(hgrn_lb_logits.astype(jnp.float32), axis=1), axis=1)

    for layer in range(DEPTH):
        last = layer == DEPTH - 1
        mod = jax.nn.silu(c) @ w_ada[layer] + b_ada[layer]
        mod_c = jax.nn.silu(c_ctx) @ w_ada[layer] + b_ada[layer]
        sh1, sc1, g1, sh2, sc2, g2 = [m[:, None, :] for m in jnp.split(mod, 6, axis=-1)]
        csh1, csc1, cg1, csh2, csc2, cg2 = jnp.split(mod_c, 6, axis=-1)
        lb_fwd, lb_bwd = lower_bounds[0, layer], lower_bounds[1, layer]

        h = _modulate(x, norm_mix_w[layer], sh1, sc1)
        hc = _modulate(ctx, norm_mix_w[layer], csh1, csc1)
        p = h @ w_in[layer]
        pc = hc @ (w_in[layer, :, :CTX_COLS] if last else w_in[layer])

        cf_fwd, cf_bwd, c_inp, c_k, c_v = _split_ctx_side(pc)
        s0 = jnp.zeros((ctx.shape[0], HG_HEADS, HG_DIM, HG_DIM), jnp.float32)
        c_q_hg = None if last else _hgrn_query(_split_query_side(pc)[0])
        co_fwd, cs_fwd = _hgrn_direction(c_q_hg, cf_fwd, c_inp, lb_fwd, s0, False)
        co_bwd, cs_bwd = _hgrn_direction(c_q_hg, cf_bwd, c_inp, lb_bwd, s0, True)
        ck = _rmsnorm(_heads(c_k, ATT_KV_HEADS, HEAD_DIM), k_norm_w[layer])
        cv = _heads(c_v, ATT_KV_HEADS, HEAD_DIM)

        f_fwd, f_bwd, inp, k_raw, v_raw = _split_ctx_side(p)
        q_hg_raw, g_hg, q_raw, gates = _split_query_side(p)
        q_hg = _hgrn_query(q_hg_raw)
        o_fwd, _ = _hgrn_direction(q_hg, f_fwd, inp, lb_fwd, cs_fwd, False)
        o_bwd, _ = _hgrn_direction(q_hg, f_bwd, inp, lb_bwd, cs_bwd, True)
        y_hg = _hgrn_readout(o_fwd + o_bwd, g_hg, hgrn_norm_w[layer], x.dtype)
        q = _axial_rope(_rmsnorm(_heads(q_raw, ATT_HEADS, HEAD_DIM), q_norm_w[layer]), rows, cols)
        k = _axial_rope(_rmsnorm(_heads(k_raw, ATT_KV_HEADS, HEAD_DIM), k_norm_w[layer]), rows, cols)
        y_at = _window_attention(q, k, _heads(v_raw, ATT_KV_HEADS, HEAD_DIM), ck, cv, attn_sinks[layer])
        x_new = x + g1 * _merge(y_hg, y_at, gates, w_branch_hgrn[layer], w_branch_attn[layer], w_out[layer])
        x_new = x_new + g2 * _swiglu(_modulate(x_new, norm_ffn_w[layer], sh2, sc2),
                                     w_ffn_gate[layer], w_ffn_up[layer], w_ffn_down[layer])

        if not last:
            _, cg_hg, cq_raw, c_gates = _split_query_side(pc)
            cy_hg = _hgrn_readout(co_fwd + co_bwd, cg_hg, hgrn_norm_w[layer], ctx.dtype)
            cq = _rmsnorm(_heads(cq_raw, ATT_HEADS, HEAD_DIM), q_norm_w[layer])
            cy_at = _context_attention(cq, ck, cv, attn_sinks[layer])
            ctx = ctx + cg1 * _merge(cy_hg, cy_at, c_gates, w_branch_hgrn[layer],
                                     w_branch_attn[layer], w_out[layer])
            ctx = ctx + cg2 * _swiglu(_modulate(ctx, norm_ffn_w[layer], csh2, csc2),
                                      w_ffn_gate[layer], w_ffn_up[layer], w_ffn_down[layer])
        x = x_new
    return x


import jax as _jax
import jax.numpy as _jnp

TWIN_FORMAT = 'train_step'
FWD_PARAMS = ['x', 'c', 'ctx', 'c_ctx', 'w_ada', 'b_ada', 'norm_mix_w', 'norm_ffn_w', 'w_in', 'hgrn_lb_logits', 'hgrn_norm_w', 'q_norm_w', 'k_norm_w', 'attn_sinks', 'w_branch_hgrn', 'w_branch_attn', 'w_out', 'w_ffn_gate', 'w_ffn_up', 'w_ffn_down']
TWIN_WEIGHTS = ['c_ctx', 'w_ada', 'b_ada', 'norm_mix_w', 'norm_ffn_w', 'w_in', 'hgrn_lb_logits', 'hgrn_norm_w', 'q_norm_w', 'k_norm_w', 'attn_sinks', 'w_branch_hgrn', 'w_branch_attn', 'w_out', 'w_ffn_gate', 'w_ffn_up', 'w_ffn_down']
TWIN_DIFF_INPUT = 'x'
TWIN_INPUTS = ['x', 'c', 'ctx', 'c_ctx', 'w_ada', 'b_ada', 'norm_mix_w', 'norm_ffn_w', 'w_in', 'hgrn_lb_logits', 'hgrn_norm_w', 'q_norm_w', 'k_norm_w', 'attn_sinks', 'w_branch_hgrn', 'w_branch_attn', 'w_out', 'w_ffn_gate', 'w_ffn_up', 'w_ffn_down', 'loss_target', 'm_c_ctx', 'm_w_ada', 'm_b_ada', 'm_norm_mix_w', 'm_norm_ffn_w', 'm_w_in', 'm_hgrn_lb_logits', 'm_hgrn_norm_w', 'm_q_norm_w', 'm_k_norm_w', 'm_attn_sinks', 'm_w_branch_hgrn', 'm_w_branch_attn', 'm_w_out', 'm_w_ffn_gate', 'm_w_ffn_up', 'm_w_ffn_down', 'v_c_ctx', 'v_w_ada', 'v_b_ada', 'v_norm_mix_w', 'v_norm_ffn_w', 'v_w_in', 'v_hgrn_lb_logits', 'v_hgrn_norm_w', 'v_q_norm_w', 'v_k_norm_w', 'v_attn_sinks', 'v_w_branch_hgrn', 'v_w_branch_attn', 'v_w_out', 'v_w_ffn_gate', 'v_w_ffn_up', 'v_w_ffn_down']
TWIN_OUTPUTS = ['loss', 'grad_x', 'grad_c_ctx', 'grad_w_ada', 'grad_b_ada', 'grad_norm_mix_w', 'grad_norm_ffn_w', 'grad_w_in', 'grad_hgrn_lb_logits', 'grad_hgrn_norm_w', 'grad_q_norm_w', 'grad_k_norm_w', 'grad_attn_sinks', 'grad_w_branch_hgrn', 'grad_w_branch_attn', 'grad_w_out', 'grad_w_ffn_gate', 'grad_w_ffn_up', 'grad_w_ffn_down', 'delta_c_ctx', 'delta_w_ada', 'delta_b_ada', 'delta_norm_mix_w', 'delta_norm_ffn_w', 'delta_w_in', 'delta_hgrn_lb_logits', 'delta_hgrn_norm_w', 'delta_q_norm_w', 'delta_k_norm_w', 'delta_attn_sinks', 'delta_w_branch_hgrn', 'delta_w_branch_attn', 'delta_w_out', 'delta_w_ffn_gate', 'delta_w_ffn_up', 'delta_w_ffn_down', 'new_m_c_ctx', 'new_m_w_ada', 'new_m_b_ada', 'new_m_norm_mix_w', 'new_m_norm_ffn_w', 'new_m_w_in', 'new_m_hgrn_lb_logits', 'new_m_hgrn_norm_w', 'new_m_q_norm_w', 'new_m_k_norm_w', 'new_m_attn_sinks', 'new_m_w_branch_hgrn', 'new_m_w_branch_attn', 'new_m_w_out', 'new_m_w_ffn_gate', 'new_m_w_ffn_up', 'new_m_w_ffn_down', 'new_v_c_ctx', 'new_v_w_ada', 'new_v_b_ada', 'new_v_norm_mix_w', 'new_v_norm_ffn_w', 'new_v_w_in', 'new_v_hgrn_lb_logits', 'new_v_hgrn_norm_w', 'new_v_q_norm_w', 'new_v_k_norm_w', 'new_v_attn_sinks', 'new_v_w_branch_hgrn', 'new_v_w_branch_attn', 'new_v_w_out', 'new_v_w_ffn_gate', 'new_v_w_ffn_up', 'new_v_w_ffn_down']
TWIN_LEAF_KINDS = {'loss': 'loss', 'grad_x': 'grad_x', 'grad_c_ctx': 'grad_w', 'grad_w_ada': 'grad_w', 'grad_b_ada': 'grad_w', 'grad_norm_mix_w': 'grad_w', 'grad_norm_ffn_w': 'grad_w', 'grad_w_in': 'grad_w', 'grad_hgrn_lb_logits': 'grad_w', 'grad_hgrn_norm_w': 'grad_w', 'grad_q_norm_w': 'grad_w', 'grad_k_norm_w': 'grad_w', 'grad_attn_sinks': 'grad_w', 'grad_w_branch_hgrn': 'grad_w', 'grad_w_branch_attn': 'grad_w', 'grad_w_out': 'grad_w', 'grad_w_ffn_gate': 'grad_w', 'grad_w_ffn_up': 'grad_w', 'grad_w_ffn_down': 'grad_w', 'delta_c_ctx': 'delta_w', 'delta_w_ada': 'delta_w', 'delta_b_ada': 'delta_w', 'delta_norm_mix_w': 'delta_w', 'delta_norm_ffn_w': 'delta_w', 'delta_w_in': 'delta_w', 'delta_hgrn_lb_logits': 'delta_w', 'delta_hgrn_norm_w': 'delta_w', 'delta_q_norm_w': 'delta_w', 'delta_k_norm_w': 'delta_w', 'delta_attn_sinks': 'delta_w', 'delta_w_branch_hgrn': 'delta_w', 'delta_w_branch_attn': 'delta_w', 'delta_w_out': 'delta_w', 'delta_w_ffn_gate': 'delta_w', 'delta_w_ffn_up': 'delta_w', 'delta_w_ffn_down': 'delta_w', 'new_m_c_ctx': 'new_m', 'new_m_w_ada': 'new_m', 'new_m_b_ada': 'new_m', 'new_m_norm_mix_w': 'new_m', 'new_m_norm_ffn_w': 'new_m', 'new_m_w_in': 'new_m', 'new_m_hgrn_lb_logits': 'new_m', 'new_m_hgrn_norm_w': 'new_m', 'new_m_q_norm_w': 'new_m', 'new_m_k_norm_w': 'new_m', 'new_m_attn_sinks': 'new_m', 'new_m_w_branch_hgrn': 'new_m', 'new_m_w_branch_attn': 'new_m', 'new_m_w_out': 'new_m', 'new_m_w_ffn_gate': 'new_m', 'new_m_w_ffn_up': 'new_m', 'new_m_w_ffn_down': 'new_m', 'new_v_c_ctx': 'new_v', 'new_v_w_ada': 'new_v', 'new_v_b_ada': 'new_v', 'new_v_norm_mix_w': 'new_v', 'new_v_norm_ffn_w': 'new_v', 'new_v_w_in': 'new_v', 'new_v_hgrn_lb_logits': 'new_v', 'new_v_hgrn_norm_w': 'new_v', 'new_v_q_norm_w': 'new_v', 'new_v_k_norm_w': 'new_v', 'new_v_attn_sinks': 'new_v', 'new_v_w_branch_hgrn': 'new_v', 'new_v_w_branch_attn': 'new_v', 'new_v_w_out': 'new_v', 'new_v_w_ffn_gate': 'new_v', 'new_v_w_ffn_up': 'new_v', 'new_v_w_ffn_down': 'new_v'}


def _forward(args):
    return _fwd_reference(*[args[k] for k in FWD_PARAMS])


def _output_shape():
    out = _jax.eval_shape(lambda: _forward(_fwd_setup_inputs(0)))
    return out.shape, out.dtype

N_MICROBATCH = 1
ADAM_LR = 0.001
ADAM_B1 = 0.9
ADAM_B2 = 0.999
ADAM_EPS = 1e-08
ADAM_WD = 0.01
ADAM_STEP = 10
PER_EXAMPLE_BATCH_AXIS = {'x': 0, 'c': 0, 'ctx': 0, 'loss_target': 0}
SHARED_INPUTS = []
_WEIGHT_DTYPES = {'c_ctx': _jnp.float32, 'w_ada': _jnp.float32, 'b_ada': _jnp.float32, 'norm_mix_w': _jnp.float32, 'norm_ffn_w': _jnp.float32, 'w_in': _jnp.float32, 'hgrn_lb_logits': _jnp.float32, 'hgrn_norm_w': _jnp.float32, 'q_norm_w': _jnp.float32, 'k_norm_w': _jnp.float32, 'attn_sinks': _jnp.float32, 'w_branch_hgrn': _jnp.float32, 'w_branch_attn': _jnp.float32, 'w_out': _jnp.float32, 'w_ffn_gate': _jnp.float32, 'w_ffn_up': _jnp.float32, 'w_ffn_down': _jnp.float32}
MOMENT_SCALE = {'c_ctx': 3.352700e-02, 'w_ada': 4.270574e-01, 'b_ada': 9.349439e-01, 'norm_mix_w': 2.401510e-01, 'norm_ffn_w': 1.735065e+00, 'w_in': 3.278416e-02, 'hgrn_lb_logits': 1.573622e-03, 'hgrn_norm_w': 1.601659e+00, 'q_norm_w': 3.425586e-02, 'k_norm_w': 3.518600e-02, 'attn_sinks': 4.412515e-03, 'w_branch_hgrn': 3.198563e-02, 'w_branch_attn': 2.688092e-02, 'w_out': 3.774843e-02, 'w_ffn_gate': 4.481528e-02, 'w_ffn_up': 3.277084e-02, 'w_ffn_down': 5.065599e-02}


def _to_microbatches(a, axis):
    t = _jnp.moveaxis(a, axis, 0)
    t = t.reshape((N_MICROBATCH, t.shape[0] // N_MICROBATCH) + t.shape[1:])
    return _jnp.moveaxis(t, 1, axis + 1)


def setup_inputs(seed: int = 0) -> dict:
    inp = _fwd_setup_inputs(seed)
    key = _jax.random.fold_in(_jax.random.key(seed), 7919)
    shape, _ = _output_shape()
    out = dict(inp)
    out["loss_target"] = _jax.random.normal(_jax.random.fold_in(key, 0), shape, _jnp.float32)
    for i, name in enumerate(TWIN_WEIGHTS):
        w = inp[name].astype(_jnp.float32)
        if MOMENT_SCALE is None:
            s = _jnp.sqrt(_jnp.mean(_jnp.square(w)) + 1e-30)
        else:
            s = MOMENT_SCALE[name]
        km, kv = _jax.random.split(_jax.random.fold_in(key, i + 1))
        out[name] = w
        out["m_" + name] = s * _jax.random.normal(km, w.shape, _jnp.float32)
        out["v_" + name] = (s * s) * _jax.random.uniform(kv, w.shape, _jnp.float32, 0.5, 1.5)
    if N_MICROBATCH > 1:
        for name, axis in PER_EXAMPLE_BATCH_AXIS.items():
            out[name] = _to_microbatches(out[name], axis)
    return {'x': out['x'], 'c': out['c'], 'ctx': out['ctx'], 'c_ctx': out['c_ctx'], 'w_ada': out['w_ada'], 'b_ada': out['b_ada'], 'norm_mix_w': out['norm_mix_w'], 'norm_ffn_w': out['norm_ffn_w'], 'w_in': out['w_in'], 'hgrn_lb_logits': out['hgrn_lb_logits'], 'hgrn_norm_w': out['hgrn_norm_w'], 'q_norm_w': out['q_norm_w'], 'k_norm_w': out['k_norm_w'], 'attn_sinks': out['attn_sinks'], 'w_branch_hgrn': out['w_branch_hgrn'], 'w_branch_attn': out['w_branch_attn'], 'w_out': out['w_out'], 'w_ffn_gate': out['w_ffn_gate'], 'w_ffn_up': out['w_ffn_up'], 'w_ffn_down': out['w_ffn_down'], 'loss_target': out['loss_target'], 'm_c_ctx': out['m_c_ctx'], 'm_w_ada': out['m_w_ada'], 'm_b_ada': out['m_b_ada'], 'm_norm_mix_w': out['m_norm_mix_w'], 'm_norm_ffn_w': out['m_norm_ffn_w'], 'm_w_in': out['m_w_in'], 'm_hgrn_lb_logits': out['m_hgrn_lb_logits'], 'm_hgrn_norm_w': out['m_hgrn_norm_w'], 'm_q_norm_w': out['m_q_norm_w'], 'm_k_norm_w': out['m_k_norm_w'], 'm_attn_sinks': out['m_attn_sinks'], 'm_w_branch_hgrn': out['m_w_branch_hgrn'], 'm_w_branch_attn': out['m_w_branch_attn'], 'm_w_out': out['m_w_out'], 'm_w_ffn_gate': out['m_w_ffn_gate'], 'm_w_ffn_up': out['m_w_ffn_up'], 'm_w_ffn_down': out['m_w_ffn_down'], 'v_c_ctx': out['v_c_ctx'], 'v_w_ada': out['v_w_ada'], 'v_b_ada': out['v_b_ada'], 'v_norm_mix_w': out['v_norm_mix_w'], 'v_norm_ffn_w': out['v_norm_ffn_w'], 'v_w_in': out['v_w_in'], 'v_hgrn_lb_logits': out['v_hgrn_lb_logits'], 'v_hgrn_norm_w': out['v_hgrn_norm_w'], 'v_q_norm_w': out['v_q_norm_w'], 'v_k_norm_w': out['v_k_norm_w'], 'v_attn_sinks': out['v_attn_sinks'], 'v_w_branch_hgrn': out['v_w_branch_hgrn'], 'v_w_branch_attn': out['v_w_branch_attn'], 'v_w_out': out['v_w_out'], 'v_w_ffn_gate': out['v_w_ffn_gate'], 'v_w_ffn_up': out['v_w_ffn_up'], 'v_w_ffn_down': out['v_w_ffn_down']}


def _loss(weights, diff, rest, loss_target):
    with _jax.named_scope("forward"):
        args = {**rest, TWIN_DIFF_INPUT: diff, **{k: w.astype(_WEIGHT_DTYPES[k]) for k, w in weights.items()}}
        y = _forward(args)
    with _jax.named_scope("loss_head"):
        err = _jnp.square(y.astype(_jnp.float32) - loss_target)
        return 0.5 * _jnp.sum(_jnp.mean(err, axis=-1)) if err.ndim else 0.5 * err


def _adamw(w, g, m, v):
    m = ADAM_B1 * m + (1.0 - ADAM_B1) * g
    v = ADAM_B2 * v + (1.0 - ADAM_B2) * _jnp.square(g)
    m_hat = m / (1.0 - ADAM_B1 ** ADAM_STEP)
    v_hat = v / (1.0 - ADAM_B2 ** ADAM_STEP)
    delta = -ADAM_LR * (m_hat / (_jnp.sqrt(v_hat) + ADAM_EPS) + ADAM_WD * w)
    return delta, m, v


def reference(x, c, ctx, c_ctx, w_ada, b_ada, norm_mix_w, norm_ffn_w, w_in, hgrn_lb_logits, hgrn_norm_w, q_norm_w, k_norm_w, attn_sinks, w_branch_hgrn, w_branch_attn, w_out, w_ffn_gate, w_ffn_up, w_ffn_down, loss_target, m_c_ctx, m_w_ada, m_b_ada, m_norm_mix_w, m_norm_ffn_w, m_w_in, m_hgrn_lb_logits, m_hgrn_norm_w, m_q_norm_w, m_k_norm_w, m_attn_sinks, m_w_branch_hgrn, m_w_branch_attn, m_w_out, m_w_ffn_gate, m_w_ffn_up, m_w_ffn_down, v_c_ctx, v_w_ada, v_b_ada, v_norm_mix_w, v_norm_ffn_w, v_w_in, v_hgrn_lb_logits, v_hgrn_norm_w, v_q_norm_w, v_k_norm_w, v_attn_sinks, v_w_branch_hgrn, v_w_branch_attn, v_w_out, v_w_ffn_gate, v_w_ffn_up, v_w_ffn_down):
    given = dict(x=x, c=c, ctx=ctx, c_ctx=c_ctx, w_ada=w_ada, b_ada=b_ada, norm_mix_w=norm_mix_w, norm_ffn_w=norm_ffn_w, w_in=w_in, hgrn_lb_logits=hgrn_lb_logits, hgrn_norm_w=hgrn_norm_w, q_norm_w=q_norm_w, k_norm_w=k_norm_w, attn_sinks=attn_sinks, w_branch_hgrn=w_branch_hgrn, w_branch_attn=w_branch_attn, w_out=w_out, w_ffn_gate=w_ffn_gate, w_ffn_up=w_ffn_up, w_ffn_down=w_ffn_down, loss_target=loss_target, m_c_ctx=m_c_ctx, m_w_ada=m_w_ada, m_b_ada=m_b_ada, m_norm_mix_w=m_norm_mix_w, m_norm_ffn_w=m_norm_ffn_w, m_w_in=m_w_in, m_hgrn_lb_logits=m_hgrn_lb_logits, m_hgrn_norm_w=m_hgrn_norm_w, m_q_norm_w=m_q_norm_w, m_k_norm_w=m_k_norm_w, m_attn_sinks=m_attn_sinks, m_w_branch_hgrn=m_w_branch_hgrn, m_w_branch_attn=m_w_branch_attn, m_w_out=m_w_out, m_w_ffn_gate=m_w_ffn_gate, m_w_ffn_up=m_w_ffn_up, m_w_ffn_down=m_w_ffn_down, v_c_ctx=v_c_ctx, v_w_ada=v_w_ada, v_b_ada=v_b_ada, v_norm_mix_w=v_norm_mix_w, v_norm_ffn_w=v_norm_ffn_w, v_w_in=v_w_in, v_hgrn_lb_logits=v_hgrn_lb_logits, v_hgrn_norm_w=v_hgrn_norm_w, v_q_norm_w=v_q_norm_w, v_k_norm_w=v_k_norm_w, v_attn_sinks=v_attn_sinks, v_w_branch_hgrn=v_w_branch_hgrn, v_w_branch_attn=v_w_branch_attn, v_w_out=v_w_out, v_w_ffn_gate=v_w_ffn_gate, v_w_ffn_up=v_w_ffn_up, v_w_ffn_down=v_w_ffn_down)
    weights = {n: given[n] for n in TWIN_WEIGHTS}
    shared = {n: given[n] for n in SHARED_INPUTS}
    per_example = {n: given[n] for n in ['x', 'c', 'ctx']}
    grad_fn = _jax.value_and_grad(_loss, argnums=(0, 1))

    def one_microbatch(ex, loss_target):
        ex = dict(ex)
        diff = ex.pop(TWIN_DIFF_INPUT)
        return grad_fn(weights, diff, {**shared, **ex}, loss_target)

    if N_MICROBATCH == 1:
        loss, (grad_w, grad_x) = one_microbatch(per_example, given["loss_target"])
    else:
        def body(carry, xs):
            loss_sum, grad_sum = carry
            l_k, (gw_k, gx_k) = one_microbatch(xs[0], xs[1])
            with _jax.named_scope("update"):
                return (loss_sum + l_k, _jax.tree.map(_jnp.add, grad_sum, gw_k)), gx_k

        init = (_jnp.zeros((), _jnp.float32), _jax.tree.map(_jnp.zeros_like, weights))
        (loss, grad_w), grad_x = _jax.lax.scan(body, init, (per_example, given["loss_target"]))
    with _jax.named_scope("update"):
        delta_w, new_m, new_v = {}, {}, {}
        for n in TWIN_WEIGHTS:
            delta_w[n], new_m[n], new_v[n] = _adamw(weights[n], grad_w[n], given["m_" + n], given["v_" + n])
    return (loss, grad_x, *[grad_w[n] for n in TWIN_WEIGHTS], *[delta_w[n] for n in TWIN_WEIGHTS],
            *[new_m[n] for n in TWIN_WEIGHTS], *[new_v[n] for n in TWIN_WEIGHTS])
```

```python
import functools

import numpy as np
import jax
import jax.numpy as jnp
from jax import lax
from jax.experimental import pallas as pl
from jax.experimental.pallas import tpu as pltpu

F32 = jnp.float32
BF16 = jnp.bfloat16
HI = lax.Precision.HIGHEST
MESH = pl.DeviceIdType.MESH

D = 1024
L = 256
TM = 256
HGW = 512
HGD = 128
CH = 32
ATW = 512
HDIM = 64
BLK = 128
GRID_W = 64
DFF = 2816
NCOL = 5376
EPS = 1e-6
ROPE_THETA = 10000.0

C_FB, C_INP, C_QHG, C_FF = 0, 1, 2, 3
C_GATES = 1
C_GHG, C_QRAW = 8, 9
C_KV = 20
C_QKV = 6

ADAM_LR, ADAM_B1, ADAM_B2, ADAM_EPS, ADAM_WD, ADAM_STEP = 0.001, 0.9, 0.999, 1e-08, 0.01, 10

NN = (((1,), (0,)), ((), ()))
NT = (((1,), (1,)), ((), ()))
TN = (((0,), (0,)), ((), ()))


def _dot(a, b, dims=NN, prec=None):
    return lax.dot_general(a, b, dims, precision=prec, preferred_element_type=F32)


def _bdot(a, b, dims=NN):
    return _dot(a.astype(BF16), b.astype(BF16), dims)


def _sig(x):
    return 1.0 / (1.0 + jnp.exp(-x))


def _pcall(body, *, name, grid, in_specs, out_specs, out_shape, scratch=(), aliases=None, vmem_mb=48):
    return pl.pallas_call(
        body, name=name, grid=grid, in_specs=in_specs, out_specs=out_specs, out_shape=out_shape,
        scratch_shapes=list(scratch), input_output_aliases=aliases or {},
        compiler_params=pltpu.CompilerParams(
            dimension_semantics=("arbitrary",) * len(grid), vmem_limit_bytes=vmem_mb << 20))


def _full(shape):
    nd = len(shape)
    return pl.BlockSpec(shape, lambda *_: (0,) * nd)


ANY = pl.BlockSpec(memory_space=pl.ANY)


def _mm(a, b, *, name, mode="nn", out_dtype=F32, tm, tn, tk):
    if mode == "nn":
        (m, k), (k2, n) = a.shape, b.shape
    elif mode == "nt":
        (m, k), (n, k2) = a.shape, b.shape
    else:
        (k, m), (k2, n) = a.shape, b.shape
    assert k == k2 and m % tm == 0 and n % tn == 0 and k % tk == 0, (name, a.shape, b.shape)
    nk = k // tk
    dims = {"nn": NN, "nt": NT, "tn": TN}[mode]

    def body(a_ref, b_ref, o_ref, acc):
        kk = pl.program_id(2)

        @pl.when(kk == 0)
        def _():
            acc[...] = jnp.zeros_like(acc)

        acc[...] += _bdot(a_ref[...], b_ref[...], dims)

        @pl.when(kk == nk - 1)
        def _():
            o_ref[...] = acc[...].astype(out_dtype)

    a_spec = (pl.BlockSpec((tk, tm), lambda i, j, kk: (kk, i)) if mode == "tn"
              else pl.BlockSpec((tm, tk), lambda i, j, kk: (i, kk)))
    b_spec = (pl.BlockSpec((tn, tk), lambda i, j, kk: (j, kk)) if mode == "nt"
              else pl.BlockSpec((tk, tn), lambda i, j, kk: (kk, j)))
    return _pcall(body, name=name, grid=(m // tm, n // tn, nk), in_specs=[a_spec, b_spec],
                  out_specs=pl.BlockSpec((tm, tn), lambda i, j, kk: (i, j)),
                  out_shape=jax.ShapeDtypeStruct((m, n), out_dtype),
                  scratch=[pltpu.VMEM((tm, tn), F32)])(a, b)


def _modulate(xin, nw, ss, *, name, sel):
    rows = xin.shape[0]

    def body(x_ref, nw_ref, ss_ref, h_ref):
        x = x_ref[...]
        r = lax.rsqrt(jnp.mean(x * x, axis=-1, keepdims=True) + EPS)
        s = ss_ref[0]
        h_ref[...] = ((x * r * nw_ref[...]) * (1.0 + s[1:2]) + s[0:1]).astype(BF16)

    return _pcall(body, name=name, grid=(rows // TM,),
                  in_specs=[pl.BlockSpec((TM, D), lambda i: (i, 0)), _full((1, D)),
                            pl.BlockSpec((1, 2, D), lambda i: (sel(i), 0, 0))],
                  out_specs=pl.BlockSpec((TM, D), lambda i: (i, 0)),
                  out_shape=jax.ShapeDtypeStruct((rows, D), BF16))(xin, nw, ss)


def _norm_bwd_rows(x, dh, nw, scale):
    r = lax.rsqrt(jnp.mean(x * x, axis=-1, keepdims=True) + EPS)
    xh = x * r
    dxh = dh * ((1.0 + scale) * nw)
    dx = r * (dxh - xh * jnp.mean(dxh * xh, axis=-1, keepdims=True))
    return dx, xh


def _res1_mod2(x, ao, g1, nw2, ss2):
    s_len = x.shape[0]

    def body(x_ref, ao_ref, g_ref, nw_ref, ss_ref, x1_ref, h_ref):
        x1 = x_ref[...] + g_ref[...] * ao_ref[...]
        x1_ref[...] = x1
        r = lax.rsqrt(jnp.mean(x1 * x1, axis=-1, keepdims=True) + EPS)
        s = ss_ref[0]
        h_ref[...] = ((x1 * r * nw_ref[...]) * (1.0 + s[1:2]) + s[0:1]).astype(BF16)

    row = pl.BlockSpec((TM, D), lambda i: (i, 0))
    return _pcall(body, name="res1_mod2", grid=(s_len // TM,),
                  in_specs=[row, row, _full((1, D)), _full((1, D)), _full((1, 2, D))],
                  out_specs=[row, row],
                  out_shape=[jax.ShapeDtypeStruct((s_len, D), F32),
                             jax.ShapeDtypeStruct((s_len, D), BF16)])(x, ao, g1, nw2, ss2)


def _swiglu(ab):
    s_len = ab.shape[0]

    def body(a_ref, b_ref, z_ref):
        a = a_ref[...]
        z_ref[...] = (a * _sig(a) * b_ref[...]).astype(BF16)

    return _pcall(body, name="swiglu", grid=(s_len // TM,),
                  in_specs=[pl.BlockSpec((TM, DFF), lambda i: (i, 0)),
                            pl.BlockSpec((TM, DFF), lambda i: (i, 1))],
                  out_specs=pl.BlockSpec((TM, DFF), lambda i: (i, 0)),
                  out_shape=jax.ShapeDtypeStruct((s_len, DFF), BF16))(ab, ab)


def _swiglu_bwd(ab, dz):
    s_len = ab.shape[0]

    def body(a_ref, b_ref, dz_ref, o_ref):
        a, dz_ = a_ref[...], dz_ref[...]
        s = _sig(a)
        da = dz_ * b_ref[...] * (s * (1.0 + a * (1.0 - s)))
        db = dz_ * (a * s)
        o_ref[...] = jnp.concatenate([da, db], axis=1).astype(BF16)

    return _pcall(body, name="swiglu_bwd", grid=(s_len // TM,),
                  in_specs=[pl.BlockSpec((TM, DFF), lambda i: (i, 0)),
                            pl.BlockSpec((TM, DFF), lambda i: (i, 1)),
                            pl.BlockSpec((TM, DFF), lambda i: (i, 0))],
                  out_specs=pl.BlockSpec((TM, 2 * DFF), lambda i: (i, 0)),
                  out_shape=jax.ShapeDtypeStruct((s_len, 2 * DFF), BF16))(ab, ab, dz)


def _loss_head(x1, y, g2, tgt):
    s_len = x1.shape[0]

    def body(x1_ref, y_ref, g_ref, t_ref, sq_ref, dx2_ref, dyb_ref, dg_ref):
        i = pl.program_id(0)

        @pl.when(i == 0)
        def _():
            sq_ref[...] = jnp.zeros_like(sq_ref)
            dg_ref[...] = jnp.zeros_like(dg_ref)

        y_ = y_ref[...]
        g = g_ref[...]
        e = x1_ref[...] + g * y_ - t_ref[...]
        sq_ref[...] += jnp.sum(e * e, axis=0, keepdims=True)
        dx2 = e * (1.0 / D)
        dx2_ref[...] = dx2
        dyb_ref[...] = (g * dx2).astype(BF16)
        dg_ref[...] += jnp.sum(dx2 * y_, axis=0, keepdims=True)

    row = pl.BlockSpec((TM, D), lambda i: (i, 0))
    vec = _full((1, D))
    return _pcall(body, name="loss_head", grid=(s_len // TM,),
                  in_specs=[row, row, vec, row], out_specs=[vec, row, row, vec],
                  out_shape=[jax.ShapeDtypeStruct((1, D), F32), jax.ShapeDtypeStruct((s_len, D), F32),
                             jax.ShapeDtypeStruct((s_len, D), BF16),
                             jax.ShapeDtypeStruct((1, D), F32)])(x1, y, g2, tgt)


def _mod2_bwd(x1, dh2, dx2, ao, nw2, ss2, g1):
    s_len = x1.shape[0]

    def body(x1_ref, dh_ref, dx2_ref, ao_ref, nw_ref, ss_ref, g_ref,
             dx1_ref, da_ref, dss_ref, dnw_ref, dg_ref):
        i = pl.program_id(0)

        @pl.when(i == 0)
        def _():
            dss_ref[...] = jnp.zeros_like(dss_ref)
            dnw_ref[...] = jnp.zeros_like(dnw_ref)
            dg_ref[...] = jnp.zeros_like(dg_ref)

        dh = dh_ref[...]
        nw = nw_ref[...]
        scale = ss_ref[0][1:2]
        dxn, xh = _norm_bwd_rows(x1_ref[...], dh, nw, scale)
        dx1 = dx2_ref[...] + dxn
        dx1_ref[...] = dx1
        da_ref[...] = (g_ref[...] * dx1).astype(BF16)
        dg_ref[...] += jnp.sum(dx1 * ao_ref[...], axis=0, keepdims=True)
        dsh = jnp.sum(dh, axis=0, keepdims=True)
        dsc = jnp.sum(dh * xh * nw, axis=0, keepdims=True)
        dss_ref[...] += jnp.concatenate([dsh, dsc], axis=0)
        dnw_ref[...] += jnp.sum(dh * xh * (1.0 + scale), axis=0, keepdims=True)

    row = pl.BlockSpec((TM, D), lambda i: (i, 0))
    vec = _full((1, D))
    return _pcall(body, name="mod2_bwd", grid=(s_len // TM,),
                  in_specs=[row, row, row, row, vec, _full((1, 2, D)), vec],
                  out_specs=[row, row, _full((2, D)), vec, vec],
                  out_shape=[jax.ShapeDtypeStruct((s_len, D), F32), jax.ShapeDtypeStruct((s_len, D), BF16),
                             jax.ShapeDtypeStruct((2, D), F32), jax.ShapeDtypeStruct((1, D), F32),
                             jax.ShapeDtypeStruct((1, D), F32)])(x1, dh2, dx2, ao, nw2, ss2, g1)


def _mod1_bwd(tok, dh, dx1, nw1, ss1):
    tt = tok.shape[0]
    s_len = dx1.shape[0]

    def body(t_ref, dh_ref, dx1_ref, nw_ref, ss_ref, dx_ref, dss_ref, dnw_ref):
        i = pl.program_id(0)

        @pl.when(i == 0)
        def _():
            dnw_ref[...] = jnp.zeros_like(dnw_ref)

        @pl.when(i <= 1)
        def _():
            dss_ref[...] = jnp.zeros_like(dss_ref)

        dh_ = dh_ref[...]
        nw = nw_ref[...]
        scale = ss_ref[0][1:2]
        dxn, xh = _norm_bwd_rows(t_ref[...], dh_, nw, scale)

        @pl.when(i >= 1)
        def _():
            dx_ref[...] = dx1_ref[...] + dxn

        dsh = jnp.sum(dh_, axis=0, keepdims=True)
        dsc = jnp.sum(dh_ * xh * nw, axis=0, keepdims=True)
        dss_ref[...] += jnp.concatenate([dsh, dsc], axis=0)[None]
        dnw_ref[...] += jnp.sum(dh_ * xh * (1.0 + scale), axis=0, keepdims=True)

    row = pl.BlockSpec((TM, D), lambda i: (i, 0))
    lat = pl.BlockSpec((TM, D), lambda i: (jnp.maximum(i - 1, 0), 0))
    sel = pl.BlockSpec((1, 2, D), lambda i: (jnp.minimum(i, 1), 0, 0))
    return _pcall(body, name="mod1_bwd", grid=(tt // TM,),
                  in_specs=[row, row, lat, _full((1, D)), sel],
                  out_specs=[lat, sel, _full((1, D))],
                  out_shape=[jax.ShapeDtypeStruct((s_len, D), F32), jax.ShapeDtypeStruct((2, 2, D), F32),
                             jax.ShapeDtypeStruct((1, D), F32)])(tok, dh, dx1, nw1, ss1)


def _tri(rev, transpose=False):
    r = lax.broadcasted_iota(jnp.int32, (CH, CH), 0)
    c = lax.broadcasted_iota(jnp.int32, (CH, CH), 1)
    lower = (c >= r) if (rev != transpose) else (c <= r)
    return lower


def _hgrn_gate(fl, qraw, lg):
    lb = 1.0 / (1.0 + jnp.exp(lg[1:2] - lg[0:1]))
    sg = _sig(fl)
    f = lb + (1.0 - lb) * sg
    q = qraw * _sig(qraw) * (HGD ** -0.5)
    return lb, sg, f, q


def _hgrn_fwd(p, lg, *, rev):
    tt = p.shape[0]
    nt = tt // TM
    ncht = TM // CH
    d = 1 if rev else 0

    def tile_of(s):
        return jnp.where(s == 0, 0, nt - s) if rev else s

    def body(f_ref, inp_ref, q_ref, lg_ref, o_ref, st_ref, lf_s, k_s, q_s, state):
        s = pl.program_id(0)

        @pl.when(s == 0)
        def _():
            state[...] = jnp.zeros_like(state)

        _, _, f, q = _hgrn_gate(f_ref[...], q_ref[...], lg_ref[0])
        lf_s[...] = jnp.log(f)
        k_s[...] = 1.0 - f
        q_s[...] = q
        tri = _tri(rev)
        trif = tri.astype(F32)

        def chunk(cc, carry):
            c = (ncht - 1 - cc) if rev else cc
            r0 = pl.multiple_of(c * CH, CH)
            lf = lf_s[pl.ds(r0, CH), :]
            cum = _dot(trif, lf, prec=HI)
            tot = jnp.sum(lf, axis=0, keepdims=True)
            qd = q_s[pl.ds(r0, CH), :] * jnp.exp(cum)
            kk = k_s[pl.ds(r0, CH), :]
            kd = kk * jnp.exp(-cum)
            ke = kk * jnp.exp(tot - cum)
            et = jnp.exp(tot)
            v = inp_ref[pl.ds(r0, CH), :]
            outs = []
            for h in range(4):
                sl = slice(h * HGD, (h + 1) * HGD)
                st0 = state[h]
                st_ref[c, h] = st0
                pm = jnp.where(tri, _bdot(qd[:, sl], kd[:, sl], NT), 0.0)
                outs.append(_bdot(pm, v[:, sl]) + _bdot(qd[:, sl], st0, NT))
                state[h] = st0 * et[:, sl] + _bdot(v[:, sl], ke[:, sl], TN)
            o_ref[pl.ds(r0, CH), :] = jnp.concatenate(outs, axis=1)
            return carry

        lax.fori_loop(0, ncht, chunk, 0)

    def col(cb):
        return pl.BlockSpec((TM, HGW), lambda s: (tile_of(s), cb))

    return _pcall(
        body, name="hgrn_fwd_rev" if rev else "hgrn_fwd", grid=(nt,),
        in_specs=[col(C_FB if rev else C_FF), col(C_INP), col(C_QHG),
                  pl.BlockSpec((1, 2, HGW), lambda s: (d, 0, 0))],
        out_specs=[pl.BlockSpec((TM, HGW), lambda s: (tile_of(s), 0)),
                   pl.BlockSpec((ncht, 4, HGD, HGD), lambda s: (tile_of(s), 0, 0, 0))],
        out_shape=[jax.ShapeDtypeStruct((tt, HGW), F32),
                   jax.ShapeDtypeStruct((nt * ncht, 4, HGD, HGD), F32)],
        scratch=[pltpu.VMEM((TM, HGW), F32)] * 3 + [pltpu.VMEM((4, HGD, HGD), F32)])(p, p, p, lg)


def _hgrn_bwd(p, lg, do, st, dp, prev, *, rev):
    tt = p.shape[0]
    nt = tt // TM
    ncht = TM // CH
    d = 1 if rev else 0
    second = prev is not None

    def tile_of(s):
        return jnp.where(s == nt - 1, 0, s + 1) if rev else nt - 1 - s

    def body(*refs):
        if second:
            (f_ref, inp_ref, q_ref, lg_ref, do_ref, st_ref, dvp_ref, dqp_ref, _dp_in,
             dp_ref, dlg_ref, lf_s, k_s, q_s, do_s, dq_s, dk_s, dv_s, dlf_s, dstate) = refs
        else:
            (f_ref, inp_ref, q_ref, lg_ref, do_ref, st_ref, _dp_in,
             dp_ref, dv_ref, dq_ref, dlg_ref, lf_s, k_s, q_s, do_s, dq_s, dk_s, dv_s, dlf_s, dstate) = refs
        s = pl.program_id(0)
        tile = tile_of(s)

        @pl.when(s == 0)
        def _():
            dstate[...] = jnp.zeros_like(dstate)
            dlg_ref[...] = jnp.zeros_like(dlg_ref)

        qraw = q_ref[...]
        lb, sg, f, q = _hgrn_gate(f_ref[...], qraw, lg_ref[0])
        lf_s[...] = jnp.log(f)
        k_s[...] = 1.0 - f
        q_s[...] = q
        do_s[...] = jnp.where(tile == 0, 0.0, do_ref[...])
        tri = _tri(rev)
        trit = _tri(rev, transpose=True)
        tritf = trit.astype(F32)
        trif = tri.astype(F32)

        def chunk(cc, carry):
            c = cc if rev else (ncht - 1 - cc)
            r0 = pl.multiple_of(c * CH, CH)
            lf = lf_s[pl.ds(r0, CH), :]
            cum = _dot(trif, lf, prec=HI)
            tot = jnp.sum(lf, axis=0, keepdims=True)
            ea = jnp.exp(cum)
            eb = jnp.exp(-cum)
            ee = jnp.exp(tot - cum)
            et = jnp.exp(tot)
            qd = q_s[pl.ds(r0, CH), :] * ea
            kk = k_s[pl.ds(r0, CH), :]
            kd = kk * eb
            ke = kk * ee
            v = inp_ref[pl.ds(r0, CH), :]
            doc = do_s[pl.ds(r0, CH), :]
            dq_l, dk_l, dv_l, dcum_l, dtot_l = [], [], [], [], []
            for h in range(4):
                sl = slice(h * HGD, (h + 1) * HGD)
                qd_, kd_, ke_, v_, do_ = qd[:, sl], kd[:, sl], ke[:, sl], v[:, sl], doc[:, sl]
                st0 = st_ref[c, h]
                ds1 = dstate[h]
                pmt = jnp.where(trit, _bdot(kd_, qd_, NT), 0.0)
                dpm = jnp.where(tri, _bdot(do_, v_, NT), 0.0)
                dpmt = jnp.where(trit, _bdot(v_, do_, NT), 0.0)
                dv = _bdot(pmt, do_) + _bdot(ke_, ds1, NT)
                dqd = _bdot(dpm, kd_) + _bdot(do_, st0)
                dkd = _bdot(dpmt, qd_)
                dke = _bdot(v_, ds1)
                dstate[h] = ds1 * et[:, sl] + _bdot(do_, qd_, TN)
                dtot_l.append(jnp.sum(ds1 * st0, axis=0, keepdims=True) * et[:, sl]
                              + jnp.sum(dke * ke_, axis=0, keepdims=True))
                dq_l.append(dqd * ea[:, sl])
                dk_l.append(dkd * eb[:, sl] + dke * ee[:, sl])
                dv_l.append(dv)
                dcum_l.append(dqd * qd_ - dkd * kd_ - dke * ke_)
            dcum = jnp.concatenate(dcum_l, axis=1)
            dlf = _dot(tritf, dcum, prec=HI) + jnp.concatenate(dtot_l, axis=1)
            dq_s[pl.ds(r0, CH), :] = jnp.concatenate(dq_l, axis=1)
            dk_s[pl.ds(r0, CH), :] = jnp.concatenate(dk_l, axis=1)
            dv_s[pl.ds(r0, CH), :] = jnp.concatenate(dv_l, axis=1)
            dlf_s[pl.ds(r0, CH), :] = dlf
            return carry

        lax.fori_loop(0, ncht, chunk, 0)

        df = dlf_s[...] / f - dk_s[...]
        dfl = df * (1.0 - lb) * sg * (1.0 - sg)
        dlb = jnp.sum(df * (1.0 - sg), axis=0, keepdims=True)
        dl0 = dlb * lb * (1.0 - lb)
        dlg_ref[...] += jnp.concatenate([dl0, -dl0], axis=0)[None]
        if second:
            sq = _sig(qraw)
            dqr = (dqp_ref[...] + dq_s[...]) * (HGD ** -0.5) * (sq * (1.0 + qraw * (1.0 - sq)))
            dp_ref[...] = jnp.concatenate([dfl, dvp_ref[...] + dv_s[...], dqr], axis=1).astype(BF16)
        else:
            dp_ref[...] = dfl.astype(BF16)
            dv_ref[...] = dv_s[...]
            dq_ref[...] = dq_s[...]

    def col(cb):
        return pl.BlockSpec((TM, HGW), lambda s: (tile_of(s), cb))

    tok = pl.BlockSpec((TM, HGW), lambda s: (tile_of(s), 0))
    in_specs = [col(C_FB if rev else C_FF), col(C_INP), col(C_QHG),
                pl.BlockSpec((1, 2, HGW), lambda s: (d, 0, 0)),
                pl.BlockSpec((TM, HGW), lambda s: (jnp.maximum(tile_of(s) - 1, 0), 0)),
                pl.BlockSpec((ncht, 4, HGD, HGD), lambda s: (tile_of(s), 0, 0, 0))]
    args = [p, p, p, lg, do, st]
    dlg_spec = _full((1, 2, HGW))
    dlg_shape = jax.ShapeDtypeStruct((1, 2, HGW), F32)
    if second:
        in_specs += [tok, tok]
        args += [prev[0], prev[1]]
        out_specs = [pl.BlockSpec((TM, 3 * HGW), lambda s: (tile_of(s), 0)), dlg_spec]
        out_shape = [jax.ShapeDtypeStruct(dp.shape, BF16), dlg_shape]
    else:
        out_specs = [pl.BlockSpec((TM, HGW), lambda s: (tile_of(s), C_FB if rev else C_FF)), tok, tok, dlg_spec]
        out_shape = [jax.ShapeDtypeStruct(dp.shape, BF16), jax.ShapeDtypeStruct((tt, HGW), F32),
                     jax.ShapeDtypeStruct((tt, HGW), F32), dlg_shape]
    in_specs.append(ANY)
    args.append(dp)
    return _pcall(body, name="hgrn_bwd_rev" if rev else "hgrn_bwd", grid=(nt,),
                  in_specs=in_specs, out_specs=out_specs, out_shape=out_shape,
                  scratch=[pltpu.VMEM((TM, HGW), F32)] * 8 + [pltpu.VMEM((4, HGD, HGD), F32)],
                  aliases={len(args) - 1: 0})(*args)


def _head_rms(o, w, nheads):
    outs = []
    for h in range(nheads):
        oh = o[:, h * HGD:(h + 1) * HGD]
        outs.append(oh * lax.rsqrt(jnp.mean(oh * oh, axis=-1, keepdims=True) + EPS))
    return jnp.concatenate(outs, axis=1)


def _readout(o0, o1, p, hw4):
    s_len = o0.shape[0] - L

    def body(o0_ref, o1_ref, g_ref, w_ref, y_ref):
        xh = _head_rms(o0_ref[...] + o1_ref[...], None, 4)
        g = g_ref[...]
        y_ref[...] = (xh * w_ref[...] * (g * _sig(g))).astype(BF16)

    lat = pl.BlockSpec((TM, HGW), lambda i: (i + 1, 0))
    return _pcall(body, name="readout", grid=(s_len // TM,),
                  in_specs=[lat, lat, pl.BlockSpec((TM, HGW), lambda i: (i + 1, C_GHG)), _full((1, HGW))],
                  out_specs=pl.BlockSpec((TM, HGW), lambda i: (i, 0)),
                  out_shape=jax.ShapeDtypeStruct((s_len, HGW), BF16))(o0, o1, p, hw4)


def _readout_bwd(o0, o1, p, hw4, dy, dp):
    tt = o0.shape[0]
    s_len = tt - L

    def body(o0_ref, o1_ref, g_ref, w_ref, dy_ref, _dp_in, dp_ref, do_ref, dw_ref):
        i = pl.program_id(0)

        @pl.when(i == 0)
        def _():
            dw_ref[...] = jnp.zeros_like(dw_ref)
            dp_ref[...] = jnp.zeros_like(dp_ref)

        @pl.when(i >= 1)
        def _():
            o = o0_ref[...] + o1_ref[...]
            g = g_ref[...]
            w = w_ref[...]
            sg = _sig(g)
            dy_ = dy_ref[...]
            dsw = dy_ * (g * sg)
            outs, xhs = [], []
            for h in range(4):
                sl = slice(h * HGD, (h + 1) * HGD)
                oh = o[:, sl]
                r = lax.rsqrt(jnp.mean(oh * oh, axis=-1, keepdims=True) + EPS)
                xh = oh * r
                dxh = dsw[:, sl] * w[:, sl]
                outs.append(r * (dxh - xh * jnp.mean(dxh * xh, axis=-1, keepdims=True)))
                xhs.append(xh)
            xh = jnp.concatenate(xhs, axis=1)
            do_ref[...] = jnp.concatenate(outs, axis=1)
            dp_ref[...] = (dy_ * xh * w * (sg * (1.0 + g * (1.0 - sg)))).astype(BF16)
            dw_ref[...] += jnp.sum(dsw * xh, axis=0, keepdims=True)

    tok = pl.BlockSpec((TM, HGW), lambda i: (i, 0))
    lat = pl.BlockSpec((TM, HGW), lambda i: (jnp.maximum(i - 1, 0), 0))
    return _pcall(body, name="readout_bwd", grid=(tt // TM,),
                  in_specs=[tok, tok, pl.BlockSpec((TM, HGW), lambda i: (i, C_GHG)), _full((1, HGW)), lat, ANY],
                  out_specs=[pl.BlockSpec((TM, HGW), lambda i: (i, C_GHG)), lat, _full((1, HGW))],
                  out_shape=[jax.ShapeDtypeStruct(dp.shape, BF16), jax.ShapeDtypeStruct((s_len, HGW), F32),
                             jax.ShapeDtypeStruct((1, HGW), F32)],
                  aliases={5: 0})(o0, o1, p, hw4, dy, dp)


def _rope_tables(s_len):
    t = np.arange(s_len)
    inv = ROPE_THETA ** (-np.arange(0, 32, 2, dtype=np.float64) / 32)
    def half(pos):
        ang = pos[:, None].astype(np.float64) * inv[None, :]
        return (np.concatenate([np.cos(ang), np.cos(ang)], 1), np.concatenate([-np.sin(ang), np.sin(ang)], 1))
    cr, sr = half(t // GRID_W)
    cc, sc = half(t % GRID_W)
    cos = np.concatenate([cr, cc, cr, cc], 1)
    sin = np.concatenate([sr, sc, sr, sc], 1)
    cos = np.concatenate([np.ones((L, 128)), cos], 0)
    sin = np.concatenate([np.zeros((L, 128)), sin], 0)
    return jnp.asarray(cos, F32), jnp.asarray(sin, F32)


def _blockdiag(n, w):
    i = np.arange(n)
    return jnp.asarray((i[:, None] // w == i[None, :] // w) / float(w), F32)


def _dup_matrix():
    m = np.zeros((128, 512), np.float32)
    for g in range(2):
        for j in range(4):
            for dd in range(HDIM):
                m[64 * g + dd, 256 * g + 64 * j + dd] = 1.0
    return m


def _rot(x):
    n = x.shape[1]
    lane = lax.broadcasted_iota(jnp.int32, x.shape, 1)
    return jnp.where((lane % 32) < 16, pltpu.roll(x, n - 16, 1), pltpu.roll(x, 16, 1))


def _qk_prep(p, cos, sin, qnw8, knw2, bd512, bd128, dup):
    tt = p.shape[0]

    def body(q_ref, kv_ref, cos_ref, sin_ref, qw_ref, kw_ref, b5_ref, b1_ref, dup_ref,
             qr_ref, k4_ref, v4_ref):
        cos_, sin_ = cos_ref[...], sin_ref[...]
        q = q_ref[...]
        qn = q * lax.rsqrt(_dot(q * q, b5_ref[...], prec=HI) + EPS) * qw_ref[...]
        cos4 = jnp.concatenate([cos_] * 4, axis=1)
        sin4 = jnp.concatenate([sin_] * 4, axis=1)
        qr_ref[...] = ((qn * cos4 + _rot(qn) * sin4) * (HDIM ** -0.5)).astype(BF16)
        kv = kv_ref[...]
        k, v = kv[:, :128], kv[:, 128:]
        kn = k * lax.rsqrt(_dot(k * k, b1_ref[...], prec=HI) + EPS) * kw_ref[...]
        kr = kn * cos_ + _rot(kn) * sin_
        k4_ref[...] = _bdot(kr, dup_ref[...]).astype(BF16)
        v4_ref[...] = _bdot(v, dup_ref[...]).astype(BF16)

    row = lambda w, cb: pl.BlockSpec((TM, w), lambda i: (i, cb))
    out = jax.ShapeDtypeStruct((tt, ATW), BF16)
    return _pcall(body, name="qk_prep", grid=(tt // TM,),
                  in_specs=[row(ATW, C_QRAW), row(256, C_KV), row(128, 0), row(128, 0),
                            _full((1, ATW)), _full((1, 128)), _full((ATW, ATW)), _full((128, 128)),
                            _full((128, ATW))],
                  out_specs=[row(ATW, 0)] * 3, out_shape=[out] * 3)(
                      p, p, cos, sin, qnw8, knw2, bd512, bd128, dup)


def _attn_masks(i, nb):
    r = lax.broadcasted_iota(jnp.int32, (BLK, 3 * BLK + L), 0)
    c = lax.broadcasted_iota(jnp.int32, (BLK, 3 * BLK + L), 1)
    kpos = (i - 1) * BLK + c
    loc = (jnp.abs(c - BLK - r) <= BLK) & (kpos >= 0) & (kpos < nb * BLK)
    return loc | (c >= 3 * BLK)


def _lane_mask(j):
    lane = lax.broadcasted_iota(jnp.int32, (1, 256), 1)
    return (lane // HDIM) == j


def _attn_specs(nb):
    blk = lambda off: pl.BlockSpec((BLK, ATW), lambda i: (jnp.clip(i + off, 0, nb - 1) + 2, 0))
    ctx = pl.BlockSpec((L, ATW), lambda i: (0, 0))
    return blk, ctx


def _attn_fwd(qr, k4, v4, sinks):
    tt = qr.shape[0]
    s_len = tt - L
    nb = s_len // BLK

    def body(sk_ref, q_ref, kp, ko, kn, kc, vp, vo, vn, vc, y_ref, lse_ref):
        i = pl.program_id(0)
        valid = _attn_masks(i, nb)
        q = q_ref[...]
        ys, lses = [], []
        for g in range(2):
            gs = slice(256 * g, 256 * g + 256)
            kcat = jnp.concatenate([kp[:, gs], ko[:, gs], kn[:, gs], kc[:, gs]], axis=0)
            vcat = jnp.concatenate([vp[:, gs], vo[:, gs], vn[:, gs], vc[:, gs]], axis=0)
            qg = q[:, gs]
            og = jnp.zeros((BLK, 256), F32)
            lg = jnp.zeros((BLK, 256), F32)
            for j in range(4):
                lm = _lane_mask(j)
                sink = sk_ref[4 * g + j]
                s = jnp.where(valid, _dot(jnp.where(lm, qg, jnp.zeros_like(qg)), kcat, NT), -1e30)
                m = jnp.maximum(jnp.max(s, axis=-1, keepdims=True), sink)
                e = jnp.exp(s - m)
                den = jnp.sum(e, axis=-1, keepdims=True) + jnp.exp(sink - m)
                pr = e / den
                og = og + jnp.where(lm, _bdot(pr, vcat), 0.0)
                lg = lg + jnp.where(lm, m + jnp.log(den), 0.0)
            ys.append(og)
            lses.append(lg)
        y_ref[...] = jnp.concatenate(ys, axis=1).astype(BF16)
        lse_ref[...] = jnp.concatenate(lses, axis=1)

    blk, ctx = _attn_specs(nb)
    out = pl.BlockSpec((BLK, ATW), lambda i: (i, 0))
    return _pcall(body, name="attn_fwd", grid=(nb,),
                  in_specs=[pl.BlockSpec(memory_space=pltpu.SMEM), blk(0),
                            blk(-1), blk(0), blk(1), ctx, blk(-1), blk(0), blk(1), ctx],
                  out_specs=[out, out],
                  out_shape=[jax.ShapeDtypeStruct((s_len, ATW), BF16),
                             jax.ShapeDtypeStruct((s_len, ATW), F32)])(
                      sinks, qr, k4, k4, k4, k4, v4, v4, v4, v4)


def _attn_bwd(qr, k4, v4, sinks, y, lse, dy):
    tt = qr.shape[0]
    s_len = tt - L
    nb = s_len // BLK

    def body(sk_ref, q_ref, kp, ko, kn, kc, vp, vo, vn, vc, y_ref, lse_ref, dy_ref,
             dq_ref, dkw_ref, dvw_ref, dkc_ref, dvc_ref, dsk_ref):
        i = pl.program_id(0)

        @pl.when(i == 0)
        def _():
            dkc_ref[...] = jnp.zeros_like(dkc_ref)
            dvc_ref[...] = jnp.zeros_like(dvc_ref)
            dsk_ref[...] = jnp.zeros_like(dsk_ref)

        valid = _attn_masks(i, nb)
        q = q_ref[...]
        dy_ = dy_ref[...]
        dly = dy_ * y_ref[...].astype(F32)
        lse_ = lse_ref[...]
        dqs = []
        for g in range(2):
            gs = slice(256 * g, 256 * g + 256)
            kcat = jnp.concatenate([kp[:, gs], ko[:, gs], kn[:, gs], kc[:, gs]], axis=0)
            vcat = jnp.concatenate([vp[:, gs], vo[:, gs], vn[:, gs], vc[:, gs]], axis=0)
            qg, dyg, dlg, lsg = q[:, gs], dy_[:, gs], dly[:, gs], lse_[:, gs]
            dqg = jnp.zeros((BLK, 256), F32)
            dkg = jnp.zeros((3 * BLK + L, 256), F32)
            dvg = jnp.zeros((3 * BLK + L, 256), F32)
            for j in range(4):
                lm = _lane_mask(j)
                sink = sk_ref[4 * g + j]
                qm = jnp.where(lm, qg, jnp.zeros_like(qg))
                dym = jnp.where(lm, dyg, 0.0).astype(BF16)
                lse_h = jnp.max(jnp.where(lm, lsg, -1e30), axis=-1, keepdims=True)
                delta = jnp.sum(jnp.where(lm, dlg, 0.0), axis=-1, keepdims=True)
                s = _dot(qm, kcat, NT)
                pr = jnp.where(valid, jnp.exp(s - lse_h), 0.0)
                dpr = _dot(dym, vcat, NT)
                dsc = pr * (dpr - delta)
                psink = jnp.exp(sink - lse_h)
                dsk_ref[4 * g + j:4 * g + j + 1, :] += jnp.broadcast_to(
                    -jnp.sum(psink * delta, axis=0, keepdims=True), (1, 128))
                dsb = dsc.astype(BF16)
                dqg = dqg + jnp.where(lm, _dot(dsb, kcat), 0.0)
                dkg = dkg + _dot(dsb, qm, TN)
                dvg = dvg + _dot(pr.astype(BF16), dym, TN)
            dqs.append(dqg)
            dkw_ref[0, :, gs] = dkg[:3 * BLK]
            dvw_ref[0, :, gs] = dvg[:3 * BLK]
            dkc_ref[:, gs] += dkg[3 * BLK:]
            dvc_ref[:, gs] += dvg[3 * BLK:]
        dq_ref[...] = jnp.concatenate(dqs, axis=1)

    blk, ctx = _attn_specs(nb)
    out = pl.BlockSpec((BLK, ATW), lambda i: (i, 0))
    win = pl.BlockSpec((1, 3 * BLK, ATW), lambda i: (i, 0, 0))
    acc = _full((L, ATW))
    return _pcall(body, name="attn_bwd", grid=(nb,),
                  in_specs=[pl.BlockSpec(memory_space=pltpu.SMEM), blk(0),
                            blk(-1), blk(0), blk(1), ctx, blk(-1), blk(0), blk(1), ctx, out, out, out],
                  out_specs=[out, win, win, acc, acc, _full((8, 128))],
                  out_shape=[jax.ShapeDtypeStruct((s_len, ATW), F32),
                             jax.ShapeDtypeStruct((nb, 3 * BLK, ATW), F32),
                             jax.ShapeDtypeStruct((nb, 3 * BLK, ATW), F32),
                             jax.ShapeDtypeStruct((L, ATW), F32), jax.ShapeDtypeStruct((L, ATW), F32),
                             jax.ShapeDtypeStruct((8, 128), F32)])(
                      sinks, qr, k4, k4, k4, k4, v4, v4, v4, v4, y, lse, dy)


def _attn_post(p, cos, sin, qnw8, knw2, bd512, bd128, dupt, dq, dkw, dvw, dkc, dvc, dp):
    tt = p.shape[0]
    s_len = tt - L
    nb = s_len // BLK
    nctx = L // BLK

    def body(q_ref, kv_ref, cos_ref, sin_ref, qw_ref, kw_ref, b5_ref, b1_ref, dupt_ref,
             dq_ref, kwp, kwo, kwn, vwp, vwo, vwn, dkc_ref, dvc_ref, _dp_in,
             dp_ref, dqw_ref, dkw_ref):
        t = pl.program_id(0)
        j = t - nctx

        @pl.when(t == 0)
        def _():
            dqw_ref[...] = jnp.zeros_like(dqw_ref)
            dkw_ref[...] = jnp.zeros_like(dkw_ref)

        is_lat = t >= nctx
        cos_, sin_ = cos_ref[...], sin_ref[...]
        has_p = is_lat & (j >= 1)
        has_n = is_lat & (j <= nb - 2)
        dk4 = (jnp.where(is_lat, kwo[0], dkc_ref[...]) + jnp.where(has_p, kwp[0], 0.0)
               + jnp.where(has_n, kwn[0], 0.0))
        dv4 = (jnp.where(is_lat, vwo[0], dvc_ref[...]) + jnp.where(has_p, vwp[0], 0.0)
               + jnp.where(has_n, vwn[0], 0.0))
        dkr = _dot(dk4, dupt_ref[...], prec=HI)
        dv = _dot(dv4, dupt_ref[...], prec=HI)
        kv = kv_ref[...]
        k = kv[:, :128]
        kw = kw_ref[...]
        rk = lax.rsqrt(_dot(k * k, b1_ref[...], prec=HI) + EPS)
        xk = k * rk
        dkn = dkr * cos_ + _rot(dkr * sin_)
        dxk = dkn * kw
        dk = rk * (dxk - xk * _dot(dxk * xk, b1_ref[...], prec=HI))
        dkw_ref[...] += jnp.sum(dkn * xk, axis=0, keepdims=True)
        q = q_ref[...]
        qw = qw_ref[...]
        rq = lax.rsqrt(_dot(q * q, b5_ref[...], prec=HI) + EPS)
        xq = q * rq
        cos4 = jnp.concatenate([cos_] * 4, axis=1)
        sin4 = jnp.concatenate([sin_] * 4, axis=1)
        dqr = jnp.where(is_lat, dq_ref[...], 0.0) * (HDIM ** -0.5)
        dqn = dqr * cos4 + _rot(dqr * sin4)
        dxq = dqn * qw
        dqraw = rq * (dxq - xq * _dot(dxq * xq, b5_ref[...], prec=HI))
        dqw_ref[...] += jnp.sum(dqn * xq, axis=0, keepdims=True)
        dp_ref[...] = jnp.concatenate([dqraw, dk, dv], axis=1).astype(BF16)

    row = lambda w, cb: pl.BlockSpec((BLK, w), lambda t: (t, cb))
    lat = pl.BlockSpec((BLK, ATW), lambda t: (jnp.maximum(t - nctx, 0), 0))

    def part(off):
        return pl.BlockSpec((1, BLK, ATW), lambda t: (jnp.clip(t - nctx + off, 0, nb - 1), 1 - off, 0))

    cacc = pl.BlockSpec((BLK, ATW), lambda t: (jnp.minimum(t, nctx - 1), 0))
    return _pcall(body, name="attn_post", grid=(tt // BLK,),
                  in_specs=[row(ATW, C_QRAW), row(256, C_KV), row(128, 0), row(128, 0),
                            _full((1, ATW)), _full((1, 128)), _full((ATW, ATW)), _full((128, 128)),
                            _full((ATW, 128)), lat, part(-1), part(0), part(1), part(-1), part(0), part(1),
                            cacc, cacc, ANY],
                  out_specs=[pl.BlockSpec((BLK, 768), lambda t: (t, C_QKV)), _full((1, ATW)), _full((1, 128))],
                  out_shape=[jax.ShapeDtypeStruct(dp.shape, BF16), jax.ShapeDtypeStruct((1, ATW), F32),
                             jax.ShapeDtypeStruct((1, 128), F32)],
                  aliases={18: 0})(p, p, cos, sin, qnw8, knw2, bd512, bd128, dupt,
                                   dq, dkw, dkw, dkw, dvw, dvw, dvw, dkc, dvc, dp)


def _merge(ah, aa, p):
    s_len = ah.shape[0]

    def body(ah_ref, aa_ref, gh_ref, ga_ref, m_ref):
        m_ref[...] = (_sig(gh_ref[...]) * ah_ref[...] + _sig(ga_ref[...]) * aa_ref[...]).astype(BF16)

    row = pl.BlockSpec((TM, D), lambda i: (i, 0))
    return _pcall(body, name="merge", grid=(s_len // TM,),
                  in_specs=[row, row, pl.BlockSpec((TM, D), lambda i: (i + 1, 2)),
                            pl.BlockSpec((TM, D), lambda i: (i + 1, 3))],
                  out_specs=row, out_shape=jax.ShapeDtypeStruct((s_len, D), BF16))(ah, aa, p, p)


def _merge_bwd(dm, ah, aa, p):
    tt = p.shape[0]
    s_len = tt - L

    def body(dm_ref, ah_ref, aa_ref, gh_ref, ga_ref, dp_ref, dmh_ref, dma_ref):
        i = pl.program_id(0)

        @pl.when(i == 0)
        def _():
            dp_ref[...] = jnp.zeros_like(dp_ref)

        @pl.when(i >= 1)
        def _():
            dm_ = dm_ref[...]
            sh, sa = _sig(gh_ref[...]), _sig(ga_ref[...])
            dp_ref[...] = jnp.concatenate([dm_ * ah_ref[...] * sh * (1.0 - sh),
                                           dm_ * aa_ref[...] * sa * (1.0 - sa)], axis=1).astype(BF16)
            dmh_ref[...] = (dm_ * sh).astype(BF16)
            dma_ref[...] = (dm_ * sa).astype(BF16)

    lat = pl.BlockSpec((TM, D), lambda i: (jnp.maximum(i - 1, 0), 0))
    return _pcall(body, name="merge_bwd", grid=(tt // TM,),
                  in_specs=[lat, lat, lat, pl.BlockSpec((TM, D), lambda i: (i, 2)),
                            pl.BlockSpec((TM, D), lambda i: (i, 3))],
                  out_specs=[pl.BlockSpec((TM, 2 * D), lambda i: (i, C_GATES)), lat, lat],
                  out_shape=[jax.ShapeDtypeStruct((tt, NCOL), BF16), jax.ShapeDtypeStruct((s_len, D), BF16),
                             jax.ShapeDtypeStruct((s_len, D), BF16)])(dm, ah, aa, p, p)


def _local_step(x, ctx, tgt, mod, modc, nw1, nw2, lg, hw, qnw, knw, sinks,
                w_in, w_bh, w_ba, w_o, w_gu, w_dn):
    s_len = x.shape[0]
    tt = s_len + L
    tok = jnp.concatenate([ctx, x], axis=0)
    ss1 = jnp.stack([modc, mod[0:2]])
    ss2 = mod[3:5][None]
    g1, g2 = mod[2:3], mod[5:6]
    hw4 = jnp.tile(hw, (1, 4))
    qnw8 = jnp.tile(qnw, (1, 8))
    knw2 = jnp.tile(knw, (1, 2))
    cos, sin = _rope_tables(s_len)
    bd512, bd128 = _blockdiag(ATW, HDIM), _blockdiag(128, HDIM)
    dupm = _dup_matrix()
    dup, dupt = jnp.asarray(dupm, BF16), jnp.asarray(dupm.T, F32)
    tmt = tt // 2 if tt % 512 else 512

    h = _modulate(tok, nw1, ss1, name="mod1", sel=lambda i: jnp.minimum(i, 1))
    p = _mm(h, w_in, name="mm_in", tm=tmt, tn=768, tk=D)
    o0, st0 = _hgrn_fwd(p, lg, rev=False)
    o1, st1 = _hgrn_fwd(p, lg, rev=True)
    y_hg = _readout(o0, o1, p, hw4)
    qr, k4, v4 = _qk_prep(p, cos, sin, qnw8, knw2, bd512, bd128, dup)
    y_at, lse = _attn_fwd(qr, k4, v4, sinks)
    ah = _mm(y_hg, w_bh, name="mm_bh", tm=512, tn=D, tk=HGW)
    aa = _mm(y_at, w_ba, name="mm_ba", tm=512, tn=D, tk=ATW)
    mixed = _merge(ah, aa, p)
    ao = _mm(mixed, w_o, name="mm_o", tm=512, tn=D, tk=D)
    x1, h2 = _res1_mod2(x, ao, g1, nw2, ss2)
    ab = _mm(h2, w_gu, name="mm_gu", tm=512, tn=1408, tk=D)
    z = _swiglu(ab)
    y = _mm(z, w_dn, name="mm_dn", tm=512, tn=D, tk=1408)
    sq, dx2, dyb, dg2 = _loss_head(x1, y, g2, tgt)

    dz = _mm(dyb, w_dn, name="mm_dz", mode="nt", tm=512, tn=1408, tk=D)
    g_dn = _mm(z, dyb, name="mm_gdn", mode="tn", tm=1408, tn=D, tk=512)
    dab = _swiglu_bwd(ab, dz)
    dh2 = _mm(dab, w_gu, name="mm_dh2", mode="nt", tm=512, tn=D, tk=1408)
    g_gu = _mm(h2, dab, name="mm_ggu", mode="tn", tm=D, tn=1408, tk=512)
    dx1, dattn, dss2, dnw2, dg1 = _mod2_bwd(x1, dh2, dx2, ao, nw2, ss2, g1)
    dm = _mm(dattn, w_o, name="mm_dm", mode="nt", tm=512, tn=D, tk=D)
    g_o = _mm(mixed, dattn, name="mm_go", mode="tn", tm=D, tn=D, tk=512)
    dp, dmh, dma = _merge_bwd(dm, ah, aa, p)
    dy_hg = _mm(dmh, w_bh, name="mm_dyh", mode="nt", tm=512, tn=HGW, tk=D)
    dy_at = _mm(dma, w_ba, name="mm_dya", mode="nt", tm=512, tn=ATW, tk=D)
    g_bh = _mm(y_hg, dmh, name="mm_gbh", mode="tn", tm=HGW, tn=D, tk=512)
    g_ba = _mm(y_at, dma, name="mm_gba", mode="tn", tm=ATW, tn=D, tk=512)
    dp, do, dhw4 = _readout_bwd(o0, o1, p, hw4, dy_hg, dp)
    dq, dkw, dvw, dkc, dvc, dsk = _attn_bwd(qr, k4, v4, sinks, y_at, lse, dy_at)
    dp, dqnw8, dknw2 = _attn_post(p, cos, sin, qnw8, knw2, bd512, bd128, dupt, dq, dkw, dvw, dkc, dvc, dp)
    dp, dv0, dq0, dlg0 = _hgrn_bwd(p, lg, do, st0, dp, None, rev=False)
    dp, dlg1 = _hgrn_bwd(p, lg, do, st1, dp, (dv0, dq0), rev=True)
    dh = _mm(dp, w_in, name="mm_dh", mode="nt", tm=tmt, tn=D, tk=768)
    g_in = _mm(h, dp, name="mm_gin", mode="tn", tm=D, tn=768, tk=tmt)
    gx, dss1, dnw1 = _mod1_bwd(tok, dh, dx1, nw1, ss1)

    dmod = jnp.concatenate([dss1[1], dg1, dss2, dg2], axis=0)
    dmodc = dss1[0]
    small = dict(dmod=dmod, dmodc=dmodc, dnw1=dnw1, dnw2=dnw2,
                 dhw=dhw4.reshape(4, HGD).sum(0, keepdims=True),
                 dqnw=dqnw8.reshape(8, HDIM).sum(0, keepdims=True),
                 dknw=dknw2.reshape(2, HDIM).sum(0, keepdims=True),
                 dsinks=dsk[:, 0], dlg=jnp.concatenate([dlg0, dlg1], axis=0))
    big = dict(w_in=g_in, w_bh=g_bh, w_ba=g_ba, w_o=g_o, w_gu=g_gu, w_dn=g_dn)
    return sq, gx, big, small


def _to_kernel_cols(w):
    return jnp.concatenate([w[..., 512:1024], w[..., 1024:1536], w[..., 1792:2304], w[..., 0:512],
                            w[..., 3328:5376], w[..., 2304:2816], w[..., 2816:3328], w[..., 1536:1792]], axis=-1)


def _from_kernel_cols(g):
    return jnp.concatenate([g[..., 1536:2048], g[..., 0:512], g[..., 512:1024], g[..., 5120:5376],
                            g[..., 1024:1536], g[..., 4096:4608], g[..., 4608:5120], g[..., 2048:4096]], axis=-1)


def _place():
    x, y, c = lax.axis_index("x"), lax.axis_index("y"), lax.axis_index("c")
    return x, y, c


def _allgather(blk, *, name, in_vmem):
    space = pltpu.VMEM if in_vmem else pl.ANY

    def body(x_ref, out_ref, send_sems, recv_sems, local_sem):
        x, y, c = _place()
        me, sibling = (x, y, c), (x, y, 1 - c)
        chips = [(1 - x, y), (x, 1 - y), (1 - x, 1 - y)]

        def slot(px, py, pc):
            return out_ref.at[4 * px + 2 * py + pc]

        def copy(k, block, to, src=None):
            return pltpu.make_async_remote_copy(
                src_ref=slot(*block) if src is None else src, dst_ref=slot(*block),
                send_sem=send_sems.at[k], recv_sem=recv_sems.at[k], device_id=to, device_id_type=MESH)

        mine = pltpu.make_async_copy(x_ref, slot(*me), local_sem)
        mine.start()
        first = [copy(0, me, sibling, src=x_ref)]
        first += [copy(1 + j, me, (*chip, c), src=x_ref) for j, chip in enumerate(chips)]
        for cp in first:
            cp.start()
        passed = [copy(4 + j, (*chip, c), sibling) for j, chip in enumerate(chips)]
        for j, chip in enumerate(chips):
            copy(1 + j, (*chip, c), me).wait_recv()
            passed[j].start()
        copy(0, sibling, me).wait_recv()
        for j, chip in enumerate(chips):
            copy(4 + j, (*chip, 1 - c), me).wait_recv()
        for cp in first + passed:
            cp.wait_send()
        mine.wait()

    return pl.pallas_call(
        body, name=name, out_shape=jax.ShapeDtypeStruct((8,) + blk.shape, blk.dtype),
        in_specs=[pl.BlockSpec(memory_space=space)], out_specs=pl.BlockSpec(memory_space=space),
        scratch_shapes=[pltpu.SemaphoreType.DMA((7,)), pltpu.SemaphoreType.DMA((7,)),
                        pltpu.SemaphoreType.DMA])(blk)


def _rs_pair_exchange(g):
    def body(g_ref, r_ref, send_sems, recv_sems):
        x, y, c = _place()
        cps = [pltpu.make_async_remote_copy(
            src_ref=g_ref.at[j, 1 - c], dst_ref=r_ref.at[j], send_sem=send_sems.at[j],
            recv_sem=recv_sems.at[j], device_id=(x, y, 1 - c), device_id_type=MESH) for j in range(4)]
        for cp in cps:
            cp.start()
        for cp in cps:
            cp.wait()

    return pl.pallas_call(
        body, name="rs_pair_exchange",
        out_shape=jax.ShapeDtypeStruct((4,) + g.shape[2:], g.dtype),
        in_specs=[ANY], out_specs=ANY,
        scratch_shapes=[pltpu.SemaphoreType.DMA((4,)), pltpu.SemaphoreType.DMA((4,))])(g)


def _rs_pair_add(g, r, c):
    rows = g.shape[2]
    tr = rows // 8

    def body(c_ref, g_ref, r_ref, o_ref):
        o_ref[...] = (g_ref[0] + r_ref[...]).astype(BF16)

    return pl.pallas_call(
        body, name="rs_pair_add",
        grid_spec=pltpu.PrefetchScalarGridSpec(
            num_scalar_prefetch=1, grid=(4, rows // tr),
            in_specs=[pl.BlockSpec((1, 1, tr, 128), lambda j, i, cr: (j, cr[0], i, 0)),
                      pl.BlockSpec((1, tr, 128), lambda j, i, cr: (j, i, 0))],
            out_specs=pl.BlockSpec((1, tr, 128), lambda j, i, cr: (j, i, 0))),
        out_shape=jax.ShapeDtypeStruct((4, rows, 128), BF16))(c.reshape(1), g, r)


def _rs_chip_exchange(pair):
    def body(p_ref, r_ref, send_sems, recv_sems, local_sem):
        x, y, c = _place()
        k = 2 * x + y
        local = pltpu.make_async_copy(p_ref.at[k], r_ref.at[k], local_sem)
        local.start()
        sends = []
        for d in range(1, 4):
            j = (k + d) % 4
            sends.append(pltpu.make_async_remote_copy(
                src_ref=p_ref.at[j], dst_ref=r_ref.at[k], send_sem=send_sems.at[d - 1],
                recv_sem=recv_sems.at[d - 1], device_id=(j // 2, j % 2, c), device_id_type=MESH))
        for cp in sends:
            cp.start()
        for d in range(1, 4):
            src = (k + 4 - d) % 4
            pltpu.make_async_remote_copy(
                src_ref=p_ref.at[src], dst_ref=r_ref.at[src], send_sem=send_sems.at[d - 1],
                recv_sem=recv_sems.at[d - 1], device_id=(x, y, c), device_id_type=MESH).wait_recv()
        for cp in sends:
            cp.wait_send()
        local.wait()

    return pl.pallas_call(
        body, name="rs_chip_exchange", out_shape=jax.ShapeDtypeStruct(pair.shape, pair.dtype),
        in_specs=[ANY], out_specs=ANY,
        scratch_shapes=[pltpu.SemaphoreType.DMA((3,)), pltpu.SemaphoreType.DMA((3,)),
                        pltpu.SemaphoreType.DMA])(pair)


def _rs_chip_add(r):
    rows = r.shape[1]
    tr = rows // 8

    def body(a, b, c_, d, o_ref):
        o_ref[...] = ((a[0].astype(F32) + b[0].astype(F32)) + c_[0].astype(F32)) + d[0].astype(F32)

    return _pcall(body, name="rs_chip_add", grid=(rows // tr,),
                  in_specs=[pl.BlockSpec((1, tr, 128), functools.partial(lambda j, i: (j, i, 0), j))
                            for j in range(4)],
                  out_specs=pl.BlockSpec((tr, 128), lambda i: (i, 0)),
                  out_shape=jax.ShapeDtypeStruct((rows, 128), F32))(r, r, r, r)


def _rs_sibling_gather(red):
    def body(a_ref, o_ref, send_sem, recv_sem, local_sem):
        x, y, c = _place()
        local = pltpu.make_async_copy(a_ref, o_ref.at[c], local_sem)
        local.start()
        cp = pltpu.make_async_remote_copy(
            src_ref=a_ref, dst_ref=o_ref.at[c], send_sem=send_sem, recv_sem=recv_sem,
            device_id=(x, y, 1 - c), device_id_type=MESH)
        cp.start()
        cp.wait()
        local.wait()

    return pl.pallas_call(
        body, name="rs_sibling_gather", out_shape=jax.ShapeDtypeStruct((2,) + red.shape, red.dtype),
        in_specs=[ANY], out_specs=ANY,
        scratch_shapes=[pltpu.SemaphoreType.DMA, pltpu.SemaphoreType.DMA, pltpu.SemaphoreType.DMA])(red)


def _ada_fwd(c16, w, b):
    n = w.shape[1]
    tn = 512

    def body(c_ref, w_ref, b_ref, o_ref):
        cc = c_ref[...]
        o_ref[...] = _dot(cc * _sig(cc), w_ref[...], prec=HI) + b_ref[...]

    return _pcall(body, name="ada_fwd", grid=(n // tn,),
                  in_specs=[_full((16, D)), pl.BlockSpec((D, tn), lambda j: (0, j)),
                            pl.BlockSpec((1, tn), lambda j: (0, j))],
                  out_specs=pl.BlockSpec((16, tn), lambda j: (0, j)),
                  out_shape=jax.ShapeDtypeStruct((16, n), F32))(c16, w, b)


def _ada_bwd(c16, dmod16, w):
    n = w.shape[1]
    tn = 512

    def body(c_ref, d_ref, w_ref, gw_ref, gc_ref):
        j = pl.program_id(0)

        @pl.when(j == 0)
        def _():
            gc_ref[...] = jnp.zeros_like(gc_ref)

        cc = c_ref[...]
        dm = d_ref[...]
        gw_ref[...] = _dot(cc * _sig(cc), dm, TN, prec=HI)
        gc_ref[...] += _dot(dm, w_ref[...], NT, prec=HI)

    return _pcall(body, name="ada_bwd", grid=(n // tn,),
                  in_specs=[_full((16, D)), pl.BlockSpec((16, tn), lambda j: (0, j)),
                            pl.BlockSpec((D, tn), lambda j: (0, j))],
                  out_specs=[pl.BlockSpec((D, tn), lambda j: (0, j)), _full((16, D))],
                  out_shape=[jax.ShapeDtypeStruct((D, n), F32),
                             jax.ShapeDtypeStruct((16, D), F32)])(c16, dmod16, w)


def _sum_devices(g):
    def body(g_ref, o_ref):
        acc = g_ref[0]
        for i in range(1, 8):
            acc = acc + g_ref[i]
        o_ref[...] = acc

    return _pcall(body, name="sum_devices", grid=(1,), in_specs=[_full(g.shape)],
                  out_specs=_full(g.shape[1:]), out_shape=jax.ShapeDtypeStruct(g.shape[1:], F32))(g)


def _cctx_grad(parts, c_ctx):
    def body(p_ref, c_ref, o_ref):
        acc = p_ref[0:1, :]
        for k in range(1, 4):
            acc = acc + p_ref[k:k + 1, :]
        cc = c_ref[...]
        s = _sig(cc)
        o_ref[...] = acc * (s * (1.0 + cc * (1.0 - s)))

    return _pcall(body, name="cctx_grad", grid=(1,), in_specs=[_full(parts.shape), _full((1, D))],
                  out_specs=_full((1, D)), out_shape=jax.ShapeDtypeStruct((1, D), F32))(parts, c_ctx)


def _adamw(w, g, m, v, *, name):
    rows, cols = w.shape
    tr = next((t for t in (256, 128, 64) if rows % t == 0), rows)
    c1 = 1.0 - ADAM_B1 ** ADAM_STEP
    c2 = 1.0 - ADAM_B2 ** ADAM_STEP

    def body(w_ref, g_ref, m_ref, v_ref, d_ref, nm_ref, nv_ref):
        g_ = g_ref[...]
        nm = ADAM_B1 * m_ref[...] + (1.0 - ADAM_B1) * g_
        nv = ADAM_B2 * v_ref[...] + (1.0 - ADAM_B2) * (g_ * g_)
        nm_ref[...] = nm
        nv_ref[...] = nv
        d_ref[...] = -ADAM_LR * ((nm / c1) / (jnp.sqrt(nv / c2) + ADAM_EPS) + ADAM_WD * w_ref[...])

    spec = pl.BlockSpec((tr, cols), lambda i: (i, 0))
    out = jax.ShapeDtypeStruct((rows, cols), F32)
    return _pcall(body, name=name, grid=(rows // tr,), in_specs=[spec] * 4, out_specs=[spec] * 3,
                  out_shape=[out] * 3)(w, g, m, v)


_BIG = [
    ("w_in", D, NCOL // 4, 1), ("w_bh", HGW, D // 4, 1), ("w_ba", ATW, D // 4, 1), ("w_o", D // 4, D, 0),
    ("w_g", D, DFF // 4, 1), ("w_u", D, DFF // 4, 1), ("w_dn", DFF // 4, D, 0)]
_PACK_ROWS = sum(r * c // 128 for _, r, c, _ in _BIG)


def _pack_shards(ws):
    return jnp.concatenate([w.reshape(-1, 128) for w in ws], axis=0)


def _unpack_shards(buf):
    out, r0 = [], 0
    for _, r, c, _ in _BIG:
        n = r * c // 128
        out.append(buf[r0:r0 + n].reshape(r, c))
        r0 += n
    return out


def _unpack_full(buf):
    out, r0 = [], 0
    for _, r, c, dim in _BIG:
        n = r * c // 128
        seg = buf[:, r0:r0 + n].reshape(4, r, c)
        out.append(seg.reshape(4 * r, c) if dim == 0 else seg.transpose(1, 0, 2).reshape(r, 4 * c))
        r0 += n
    return out


def _pack_full(gs):
    segs = []
    for (_, r, c, dim), g in zip(_BIG, gs):
        seg = g.reshape(4, r, c) if dim == 0 else g.reshape(r, 4, c).transpose(1, 0, 2)
        segs.append(seg.reshape(4, r * c // 128, 128))
    return jnp.concatenate(segs, axis=1)


def _rows128(a, rows):
    f = a.reshape(-1)
    return jnp.pad(f, (0, rows * 128 - f.shape[0])).reshape(rows, 128)


def kernel(x, c, ctx, c_ctx, w_ada, b_ada, norm_mix_w, norm_ffn_w, w_in, hgrn_lb_logits, hgrn_norm_w, q_norm_w, k_norm_w, attn_sinks, w_branch_hgrn, w_branch_attn, w_out, w_ffn_gate, w_ffn_up, w_ffn_down, loss_target, m_c_ctx, m_w_ada, m_b_ada, m_norm_mix_w, m_norm_ffn_w, m_w_in, m_hgrn_lb_logits, m_hgrn_norm_w, m_q_norm_w, m_k_norm_w, m_attn_sinks, m_w_branch_hgrn, m_w_branch_attn, m_w_out, m_w_ffn_gate, m_w_ffn_up, m_w_ffn_down, v_c_ctx, v_w_ada, v_b_ada, v_norm_mix_w, v_norm_ffn_w, v_w_in, v_hgrn_lb_logits, v_hgrn_norm_w, v_q_norm_w, v_k_norm_w, v_attn_sinks, v_w_branch_hgrn, v_w_branch_attn, v_w_out, v_w_ffn_gate, v_w_ffn_up, v_w_ffn_down):
    xi, yi, ci = _place()
    chip = 2 * xi + yi
    dev = 2 * chip + ci
    s_len = x.shape[1]

    blk = jnp.concatenate([c, _rows128(hgrn_lb_logits, 4).reshape(1, 512), jnp.zeros((1, 512), F32)], axis=1)
    blk = jnp.concatenate([blk.reshape(2, D), jnp.zeros((6, D), F32)], axis=0)
    g0 = _allgather(blk, name="ag_cond", in_vmem=True)
    c16 = jnp.concatenate([g0[:, 0], c_ctx[None], jnp.zeros((7, D), F32)], axis=0)
    lg = g0[0::2, 1, :512].reshape(4, 2, 2, 128).transpose(1, 2, 0, 3).reshape(2, 2, HGW)

    nada = w_ada.shape[2]
    b_sh = lax.dynamic_slice(b_ada, (0, chip * nada), (1, nada))
    mod_sh = _ada_fwd(c16, w_ada[0], b_sh)
    g1 = _allgather(mod_sh, name="ag_mod", in_vmem=True)
    modall = g1[0::2].transpose(1, 0, 2).reshape(16, 4 * nada)
    mod = lax.dynamic_slice(modall, (dev, 0), (1, 6 * D)).reshape(6, D)
    modc = modall[8].reshape(6, D)[:2]

    shards = [w_in[0], w_branch_hgrn[0], w_branch_attn[0], w_out[0], w_ffn_gate[0], w_ffn_up[0], w_ffn_down[0]]
    packed = _pack_shards([w.astype(BF16) for w in shards]).reshape(2, _PACK_ROWS // 2, 128)
    mine = lax.dynamic_index_in_dim(packed, ci, 0, keepdims=False)
    gw = _allgather(mine, name="ag_weights", in_vmem=False).reshape(4, _PACK_ROWS, 128)
    f_in, f_bh, f_ba, f_o, f_g, f_u, f_dn = _unpack_full(gw)

    sq, gx, big, small = _local_step(
        x[0], ctx[0], loss_target[0], mod, modc, norm_mix_w, norm_ffn_w, lg, hgrn_norm_w, q_norm_w,
        k_norm_w, attn_sinks[0], _to_kernel_cols(f_in), f_bh, f_ba, f_o, jnp.concatenate([f_g, f_u], axis=1), f_dn)
    loss = lax.psum(0.5 * jnp.sum(sq) / D, ("x", "y", "c"))

    gfull = _pack_full([_from_kernel_cols(big["w_in"]), big["w_bh"], big["w_ba"], big["w_o"],
                        big["w_gu"][:, :DFF], big["w_gu"][:, DFF:], big["w_dn"]])
    gfull = gfull.reshape(4, 2, _PACK_ROWS // 2, 128)
    recv = _rs_pair_exchange(gfull)
    pair = _rs_pair_add(gfull, recv, ci)
    contrib = _rs_chip_exchange(pair)
    red = _rs_chip_add(contrib)
    gsh = _unpack_shards(_rs_sibling_gather(red).reshape(_PACK_ROWS, 128))

    sm = jnp.concatenate([
        _rows128(small["dmod"], 48), _rows128(small["dmodc"], 16), _rows128(small["dnw1"], 8),
        _rows128(small["dnw2"], 8), _rows128(small["dhw"], 1), _rows128(small["dqnw"], 1),
        _rows128(small["dknw"], 1), _rows128(small["dsinks"], 1), _rows128(small["dlg"], 16),
        jnp.zeros((4, 128), F32)], axis=0)
    g2 = _allgather(sm, name="ag_small", in_vmem=True)
    tot = _sum_devices(g2)
    dmodc_tot = jnp.pad(tot[48:64].reshape(1, 2 * D), ((0, 0), (0, 4 * D)))
    g_b_ada = tot[0:48].reshape(1, 6 * D) + dmodc_tot
    dmod16 = jnp.concatenate([g2[:, 0:48].reshape(8, 6 * D), dmodc_tot, jnp.zeros((7, 6 * D), F32)], axis=0)
    g_w_ada, gc_part = _ada_bwd(c16, lax.dynamic_slice(dmod16, (0, chip * nada), (16, nada)), w_ada[0])
    g3 = _allgather(gc_part[8:16], name="ag_cctx", in_vmem=True)
    g_c_ctx = _cctx_grad(g3[0::2, 0], c_ctx[None])[0]
    g_nw1 = tot[64:72].reshape(1, D)
    g_nw2 = tot[72:80].reshape(1, D)
    g_hw = tot[80:81]
    g_qnw = tot[81:82, :HDIM]
    g_knw = tot[82:83, :HDIM]
    g_sinks = tot[83:84, :8]
    g_lg = lax.dynamic_slice(tot[84:100].reshape(2, 2, HGW), (0, 0, chip * 128), (2, 2, 128))

    names = ["c_ctx", "w_ada", "b_ada", "norm_mix_w", "norm_ffn_w", "w_in", "hgrn_lb_logits", "hgrn_norm_w",
             "q_norm_w", "k_norm_w", "attn_sinks", "w_branch_hgrn", "w_branch_attn", "w_out", "w_ffn_gate",
             "w_ffn_up", "w_ffn_down"]
    ws = dict(zip(names, [c_ctx, w_ada, b_ada, norm_mix_w, norm_ffn_w, w_in, hgrn_lb_logits, hgrn_norm_w,
                          q_norm_w, k_norm_w, attn_sinks, w_branch_hgrn, w_branch_attn, w_out, w_ffn_gate,
                          w_ffn_up, w_ffn_down]))
    ms = dict(zip(names, [m_c_ctx, m_w_ada, m_b_ada, m_norm_mix_w, m_norm_ffn_w, m_w_in, m_hgrn_lb_logits,
                          m_hgrn_norm_w, m_q_norm_w, m_k_norm_w, m_attn_sinks, m_w_branch_hgrn,
                          m_w_branch_attn, m_w_out, m_w_ffn_gate, m_w_ffn_up, m_w_ffn_down]))
    vs = dict(zip(names, [v_c_ctx, v_w_ada, v_b_ada, v_norm_mix_w, v_norm_ffn_w, v_w_in, v_hgrn_lb_logits,
                          v_hgrn_norm_w, v_q_norm_w, v_k_norm_w, v_attn_sinks, v_w_branch_hgrn,
                          v_w_branch_attn, v_w_out, v_w_ffn_gate, v_w_ffn_up, v_w_ffn_down]))
    grads = dict(c_ctx=g_c_ctx, w_ada=g_w_ada[None], b_ada=g_b_ada, norm_mix_w=g_nw1, norm_ffn_w=g_nw2,
                 w_in=gsh[0][None], hgrn_lb_logits=g_lg, hgrn_norm_w=g_hw, q_norm_w=g_qnw, k_norm_w=g_knw,
                 attn_sinks=g_sinks, w_branch_hgrn=gsh[1][None], w_branch_attn=gsh[2][None], w_out=gsh[3][None],
                 w_ffn_gate=gsh[4][None], w_ffn_up=gsh[5][None], w_ffn_down=gsh[6][None])
    big_names = ["w_ada", "w_in", "w_branch_hgrn", "w_branch_attn", "w_out", "w_ffn_gate", "w_ffn_up", "w_ffn_down"]
    small_names = [n for n in names if n not in big_names]
    delta, new_m, new_v = {}, {}, {}
    for n in big_names:
        d_, m_, v_ = _adamw(ws[n][0], grads[n][0], ms[n][0], vs[n][0], name="adamw_" + n)
        delta[n], new_m[n], new_v[n] = d_[None], m_[None], v_[None]
    srows = [-(-ws[n].size // 128) for n in small_names]

    def cat(d):
        return jnp.concatenate([_rows128(d[n], r) for n, r in zip(small_names, srows)], axis=0)

    sd, sm_, sv = _adamw(cat(ws), cat(grads), cat(ms), cat(vs), name="adamw_small")
    r0 = 0
    for n, r in zip(small_names, srows):
        for dst, src in ((delta, sd), (new_m, sm_), (new_v, sv)):
            dst[n] = src[r0:r0 + r].reshape(-1)[:ws[n].size].reshape(ws[n].shape)
        r0 += r
    return (loss, gx[None], *[grads[n] for n in names], *[delta[n] for n in names],
            *[new_m[n] for n in names], *[new_v[n] for n in names])
```

```python
import functools

import numpy as np
import jax
import jax.numpy as jnp
from jax import lax
from jax.experimental import pallas as pl
from jax.experimental.pallas import tpu as pltpu

F32 = jnp.float32
BF16 = jnp.bfloat16
HI = lax.Precision.HIGHEST
MESH = pl.DeviceIdType.MESH

D = 1024
L = 256
TM = 256
HGW = 512
HGD = 128
CH = 32
ATW = 512
HDIM = 64
BLK = 128
GRID_W = 64
DFF = 2816
NCOL = 5376
EPS = 1e-6
ROPE_THETA = 10000.0

C_FB, C_INP, C_QHG, C_FF = 0, 1, 2, 3
C_GATES = 1
C_GHG, C_QRAW = 8, 9
C_KV = 20
C_QKV = 6

ADAM_LR, ADAM_B1, ADAM_B2, ADAM_EPS, ADAM_WD, ADAM_STEP = 0.001, 0.9, 0.999, 1e-08, 0.01, 10

NN = (((1,), (0,)), ((), ()))
NT = (((1,), (1,)), ((), ()))
TN = (((0,), (0,)), ((), ()))


def _dot(a, b, dims=NN, prec=None):
    return lax.dot_general(a, b, dims, precision=prec, preferred_element_type=F32)


def _bdot(a, b, dims=NN):
    return _dot(a.astype(BF16), b.astype(BF16), dims)


def _sig(x):
    return 1.0 / (1.0 + jnp.exp(-x))


def _pcall(body, *, name, grid, in_specs, out_specs, out_shape, scratch=(), aliases=None, vmem_mb=48):
    return pl.pallas_call(
        body, name=name, grid=grid, in_specs=in_specs, out_specs=out_specs, out_shape=out_shape,
        scratch_shapes=list(scratch), input_output_aliases=aliases or {},
        compiler_params=pltpu.CompilerParams(
            dimension_semantics=("arbitrary",) * len(grid), vmem_limit_bytes=vmem_mb << 20))


def _full(shape):
    nd = len(shape)
    return pl.BlockSpec(shape, lambda *_: (0,) * nd)


ANY = pl.BlockSpec(memory_space=pl.ANY)


def _mm(a, b, *, name, mode="nn", out_dtype=F32, tm, tn, tk):
    if mode == "nn":
        (m, k), (k2, n) = a.shape, b.shape
    elif mode == "nt":
        (m, k), (n, k2) = a.shape, b.shape
    else:
        (k, m), (k2, n) = a.shape, b.shape
    assert k == k2 and m % tm == 0 and n % tn == 0 and k % tk == 0, (name, a.shape, b.shape)
    nk = k // tk
    dims = {"nn": NN, "nt": NT, "tn": TN}[mode]

    def body(a_ref, b_ref, o_ref, acc):
        kk = pl.program_id(2)

        @pl.when(kk == 0)
        def _():
            acc[...] = jnp.zeros_like(acc)

        acc[...] += _bdot(a_ref[...], b_ref[...], dims)

        @pl.when(kk == nk - 1)
        def _():
            o_ref[...] = acc[...].astype(out_dtype)

    a_spec = (pl.BlockSpec((tk, tm), lambda i, j, kk: (kk, i)) if mode == "tn"
              else pl.BlockSpec((tm, tk), lambda i, j, kk: (i, kk)))
    b_spec = (pl.BlockSpec((tn, tk), lambda i, j, kk: (j, kk)) if mode == "nt"
              else pl.BlockSpec((tk, tn), lambda i, j, kk: (kk, j)))
    return _pcall(body, name=name, grid=(m // tm, n // tn, nk), in_specs=[a_spec, b_spec],
                  out_specs=pl.BlockSpec((tm, tn), lambda i, j, kk: (i, j)),
                  out_shape=jax.ShapeDtypeStruct((m, n), out_dtype),
                  scratch=[pltpu.VMEM((tm, tn), F32)])(a, b)


def _modulate(xin, nw, ss, *, name, sel):
    rows = xin.shape[0]

    def body(x_ref, nw_ref, ss_ref, h_ref):
        x = x_ref[...]
        r = lax.rsqrt(jnp.mean(x * x, axis=-1, keepdims=True) + EPS)
        s = ss_ref[0]
        h_ref[...] = ((x * r * nw_ref[...]) * (1.0 + s[1:2]) + s[0:1]).astype(BF16)

    return _pcall(body, name=name, grid=(rows // TM,),
                  in_specs=[pl.BlockSpec((TM, D), lambda i: (i, 0)), _full((1, D)),
                            pl.BlockSpec((1, 2, D), lambda i: (sel(i), 0, 0))],
                  out_specs=pl.BlockSpec((TM, D), lambda i: (i, 0)),
                  out_shape=jax.ShapeDtypeStruct((rows, D), BF16))(xin, nw, ss)


def _norm_bwd_rows(x, dh, nw, scale):
    r = lax.rsqrt(jnp.mean(x * x, axis=-1, keepdims=True) + EPS)
    xh = x * r
    dxh = dh * ((1.0 + scale) * nw)
    dx = r * (dxh - xh * jnp.mean(dxh * xh, axis=-1, keepdims=True))
    return dx, xh


def _res1_mod2(x, ao, g1, nw2, ss2):
    s_len = x.shape[0]

    def body(x_ref, ao_ref, g_ref, nw_ref, ss_ref, x1_ref, h_ref):
        x1 = x_ref[...] + g_ref[...] * ao_ref[...]
        x1_ref[...] = x1
        r = lax.rsqrt(jnp.mean(x1 * x1, axis=-1, keepdims=True) + EPS)
        s = ss_ref[0]
        h_ref[...] = ((x1 * r * nw_ref[...]) * (1.0 + s[1:2]) + s[0:1]).astype(BF16)

    row = pl.BlockSpec((TM, D), lambda i: (i, 0))
    return _pcall(body, name="res1_mod2", grid=(s_len // TM,),
                  in_specs=[row, row, _full((1, D)), _full((1, D)), _full((1, 2, D))],
                  out_specs=[row, row],
                  out_shape=[jax.ShapeDtypeStruct((s_len, D), F32),
                             jax.ShapeDtypeStruct((s_len, D), BF16)])(x, ao, g1, nw2, ss2)


TS = 512


def _acc_call(body, *, name, grid, in_specs, out_specs, out_shape, acc_shapes, args):
    return _pcall(body, name=name, grid=grid, in_specs=in_specs, out_specs=out_specs, out_shape=out_shape,
                  scratch=[pltpu.VMEM(s, F32) for s in acc_shapes])(*args)


def _mm_cs(a, w4, *, name):
    m, k = a.shape
    _, _, ns = w4.shape

    def body(a_ref, w_ref, o_ref):
        o_ref[...] = _bdot(a_ref[...], w_ref[0])

    return _pcall(body, name=name, grid=(m // TS, 4),
                  in_specs=[pl.BlockSpec((TS, k), lambda i, j: (i, 0)),
                            pl.BlockSpec((1, k, ns), lambda i, j: (j, 0, 0))],
                  out_specs=pl.BlockSpec((TS, ns), lambda i, j: (i, j)),
                  out_shape=jax.ShapeDtypeStruct((m, 4 * ns), F32))(a, w4)


def _mm_cs_nt(a, w4, *, name):
    m = a.shape[0]
    _, k, ns = w4.shape

    def body(a_ref, w_ref, o_ref, acc):
        j = pl.program_id(1)

        @pl.when(j == 0)
        def _():
            acc[...] = jnp.zeros_like(acc)

        acc[...] += _bdot(a_ref[...], w_ref[0], NT)

        @pl.when(j == 3)
        def _():
            o_ref[...] = acc[...]

    return _acc_call(body, name=name, grid=(m // TS, 4),
                     in_specs=[pl.BlockSpec((TS, ns), lambda i, j: (i, j)),
                               pl.BlockSpec((1, k, ns), lambda i, j: (j, 0, 0))],
                     out_specs=pl.BlockSpec((TS, k), lambda i, j: (i, 0)),
                     out_shape=jax.ShapeDtypeStruct((m, k), F32), acc_shapes=[(TS, k)], args=(a, w4))


def _mm_cs_tn(a, b, ns, *, name):
    s_len, k = a.shape
    nk = s_len // TS

    def body(a_ref, b_ref, o_ref, acc):
        t = pl.program_id(1)

        @pl.when(t == 0)
        def _():
            acc[...] = jnp.zeros_like(acc)

        acc[...] += _bdot(a_ref[...], b_ref[...], TN)

        @pl.when(t == nk - 1)
        def _():
            o_ref[0] = acc[...]

    return _acc_call(body, name=name, grid=(4, nk),
                     in_specs=[pl.BlockSpec((TS, k), lambda j, t: (t, 0)),
                               pl.BlockSpec((TS, ns), lambda j, t: (t, j))],
                     out_specs=pl.BlockSpec((1, k, ns), lambda j, t: (j, 0, 0)),
                     out_shape=jax.ShapeDtypeStruct((4, k, ns), F32), acc_shapes=[(k, ns)], args=(a, b))


def _ffn_up(h2, g4, u4):
    s_len = h2.shape[0]
    ns = g4.shape[2]

    def body(h_ref, g_ref, u_ref, a_ref, b_ref, z_ref):
        h = h_ref[...]
        a = _bdot(h, g_ref[0])
        b = _bdot(h, u_ref[0])
        a_ref[0] = a
        b_ref[0] = b
        z_ref[0] = (a * _sig(a) * b).astype(BF16)

    w = pl.BlockSpec((1, D, ns), lambda i, j: (j, 0, 0))
    o = pl.BlockSpec((1, TS, ns), lambda i, j: (j, i, 0))
    f = jax.ShapeDtypeStruct((4, s_len, ns), F32)
    return _pcall(body, name="ffn_up", grid=(s_len // TS, 4),
                  in_specs=[pl.BlockSpec((TS, D), lambda i, j: (i, 0)), w, w], out_specs=[o, o, o],
                  out_shape=[f, f, jax.ShapeDtypeStruct((4, s_len, ns), BF16)])(h2, g4, u4)


def _ffn_down(z4, dn4):
    _, s_len, ns = z4.shape

    def body(z_ref, w_ref, o_ref, acc):
        j = pl.program_id(1)

        @pl.when(j == 0)
        def _():
            acc[...] = jnp.zeros_like(acc)

        acc[...] += _bdot(z_ref[0], w_ref[0])

        @pl.when(j == 3)
        def _():
            o_ref[...] = acc[...]

    return _acc_call(body, name="ffn_down", grid=(s_len // TS, 4),
                     in_specs=[pl.BlockSpec((1, TS, ns), lambda i, j: (j, i, 0)),
                               pl.BlockSpec((1, ns, D), lambda i, j: (j, 0, 0))],
                     out_specs=pl.BlockSpec((TS, D), lambda i, j: (i, 0)),
                     out_shape=jax.ShapeDtypeStruct((s_len, D), F32), acc_shapes=[(TS, D)], args=(z4, dn4))


def _ffn_dz(dyb, dn4, a4, b4):
    _, s_len, ns = a4.shape

    def body(dy_ref, w_ref, a_ref, b_ref, da_ref, db_ref):
        dz = _bdot(dy_ref[...], w_ref[0], NT)
        a = a_ref[0]
        s = _sig(a)
        da_ref[0] = (dz * b_ref[0] * (s * (1.0 + a * (1.0 - s)))).astype(BF16)
        db_ref[0] = (dz * (a * s)).astype(BF16)

    t = pl.BlockSpec((1, TS, ns), lambda i, j: (j, i, 0))
    o = jax.ShapeDtypeStruct((4, s_len, ns), BF16)
    return _pcall(body, name="ffn_dz", grid=(s_len // TS, 4),
                  in_specs=[pl.BlockSpec((TS, D), lambda i, j: (i, 0)),
                            pl.BlockSpec((1, ns, D), lambda i, j: (j, 0, 0)), t, t],
                  out_specs=[t, t], out_shape=[o, o])(dyb, dn4, a4, b4)


def _ffn_gdn(z4, dyb):
    _, s_len, ns = z4.shape
    nk = s_len // TS

    def body(z_ref, dy_ref, o_ref, acc):
        t = pl.program_id(1)

        @pl.when(t == 0)
        def _():
            acc[...] = jnp.zeros_like(acc)

        acc[...] += _bdot(z_ref[0], dy_ref[...], TN)

        @pl.when(t == nk - 1)
        def _():
            o_ref[0] = acc[...]

    return _acc_call(body, name="ffn_gdn", grid=(4, nk),
                     in_specs=[pl.BlockSpec((1, TS, ns), lambda j, t: (j, t, 0)),
                               pl.BlockSpec((TS, D), lambda j, t: (t, 0))],
                     out_specs=pl.BlockSpec((1, ns, D), lambda j, t: (j, 0, 0)),
                     out_shape=jax.ShapeDtypeStruct((4, ns, D), F32), acc_shapes=[(ns, D)], args=(z4, dyb))


def _ffn_dh2(da4, db4, g4, u4):
    _, s_len, ns = da4.shape

    def body(da_ref, db_ref, g_ref, u_ref, o_ref, acc):
        j = pl.program_id(1)

        @pl.when(j == 0)
        def _():
            acc[...] = jnp.zeros_like(acc)

        acc[...] += _bdot(da_ref[0], g_ref[0], NT) + _bdot(db_ref[0], u_ref[0], NT)

        @pl.when(j == 3)
        def _():
            o_ref[...] = acc[...]

    t = pl.BlockSpec((1, TS, ns), lambda i, j: (j, i, 0))
    w = pl.BlockSpec((1, D, ns), lambda i, j: (j, 0, 0))
    return _acc_call(body, name="ffn_dh2", grid=(s_len // TS, 4), in_specs=[t, t, w, w],
                     out_specs=pl.BlockSpec((TS, D), lambda i, j: (i, 0)),
                     out_shape=jax.ShapeDtypeStruct((s_len, D), F32), acc_shapes=[(TS, D)],
                     args=(da4, db4, g4, u4))


def _ffn_ggu(h2, da4, db4):
    _, s_len, ns = da4.shape
    nk = s_len // TS

    def body(h_ref, da_ref, db_ref, gg_ref, gu_ref, acc_g, acc_u):
        t = pl.program_id(1)

        @pl.when(t == 0)
        def _():
            acc_g[...] = jnp.zeros_like(acc_g)
            acc_u[...] = jnp.zeros_like(acc_u)

        h = h_ref[...]
        acc_g[...] += _bdot(h, da_ref[0], TN)
        acc_u[...] += _bdot(h, db_ref[0], TN)

        @pl.when(t == nk - 1)
        def _():
            gg_ref[0] = acc_g[...]
            gu_ref[0] = acc_u[...]

    d = pl.BlockSpec((1, TS, ns), lambda j, t: (j, t, 0))
    o = pl.BlockSpec((1, D, ns), lambda j, t: (j, 0, 0))
    f = jax.ShapeDtypeStruct((4, D, ns), F32)
    return _acc_call(body, name="ffn_ggu", grid=(4, nk),
                     in_specs=[pl.BlockSpec((TS, D), lambda j, t: (t, 0)), d, d], out_specs=[o, o],
                     out_shape=[f, f], acc_shapes=[(D, ns), (D, ns)], args=(h2, da4, db4))


def _loss_head(x1, y, g2, tgt):
    s_len = x1.shape[0]

    def body(x1_ref, y_ref, g_ref, t_ref, sq_ref, dx2_ref, dyb_ref, dg_ref):
        i = pl.program_id(0)

        @pl.when(i == 0)
        def _():
            sq_ref[...] = jnp.zeros_like(sq_ref)
            dg_ref[...] = jnp.zeros_like(dg_ref)

        y_ = y_ref[...]
        g = g_ref[...]
        e = x1_ref[...] + g * y_ - t_ref[...]
        sq_ref[...] += jnp.sum(e * e, axis=0, keepdims=True)
        dx2 = e * (1.0 / D)
        dx2_ref[...] = dx2
        dyb_ref[...] = (g * dx2).astype(BF16)
        dg_ref[...] += jnp.sum(dx2 * y_, axis=0, keepdims=True)

    row = pl.BlockSpec((TM, D), lambda i: (i, 0))
    vec = _full((1, D))
    return _pcall(body, name="loss_head", grid=(s_len // TM,),
                  in_specs=[row, row, vec, row], out_specs=[vec, row, row, vec],
                  out_shape=[jax.ShapeDtypeStruct((1, D), F32), jax.ShapeDtypeStruct((s_len, D), F32),
                             jax.ShapeDtypeStruct((s_len, D), BF16),
                             jax.ShapeDtypeStruct((1, D), F32)])(x1, y, g2, tgt)


def _mod2_bwd(x1, dh2, dx2, ao, nw2, ss2, g1):
    s_len = x1.shape[0]

    def body(x1_ref, dh_ref, dx2_ref, ao_ref, nw_ref, ss_ref, g_ref,
             dx1_ref, da_ref, dss_ref, dnw_ref, dg_ref):
        i = pl.program_id(0)

        @pl.when(i == 0)
        def _():
            dss_ref[...] = jnp.zeros_like(dss_ref)
            dnw_ref[...] = jnp.zeros_like(dnw_ref)
            dg_ref[...] = jnp.zeros_like(dg_ref)

        dh = dh_ref[...]
        nw = nw_ref[...]
        scale = ss_ref[0][1:2]
        dxn, xh = _norm_bwd_rows(x1_ref[...], dh, nw, scale)
        dx1 = dx2_ref[...] + dxn
        dx1_ref[...] = dx1
        da_ref[...] = (g_ref[...] * dx1).astype(BF16)
        dg_ref[...] += jnp.sum(dx1 * ao_ref[...], axis=0, keepdims=True)
        dsh = jnp.sum(dh, axis=0, keepdims=True)
        dsc = jnp.sum(dh * xh * nw, axis=0, keepdims=True)
        dss_ref[...] += jnp.concatenate([dsh, dsc], axis=0)
        dnw_ref[...] += jnp.sum(dh * xh * (1.0 + scale), axis=0, keepdims=True)

    row = pl.BlockSpec((TM, D), lambda i: (i, 0))
    vec = _full((1, D))
    return _pcall(body, name="mod2_bwd", grid=(s_len // TM,),
                  in_specs=[row, row, row, row, vec, _full((1, 2, D)), vec],
                  out_specs=[row, row, _full((2, D)), vec, vec],
                  out_shape=[jax.ShapeDtypeStruct((s_len, D), F32), jax.ShapeDtypeStruct((s_len, D), BF16),
                             jax.ShapeDtypeStruct((2, D), F32), jax.ShapeDtypeStruct((1, D), F32),
                             jax.ShapeDtypeStruct((1, D), F32)])(x1, dh2, dx2, ao, nw2, ss2, g1)


def _mod1_bwd(tok, dh, dx1, nw1, ss1):
    tt = tok.shape[0]
    s_len = dx1.shape[0]

    def body(t_ref, dh_ref, dx1_ref, nw_ref, ss_ref, dx_ref, dss_ref, dnw_ref):
        i = pl.program_id(0)

        @pl.when(i == 0)
        def _():
            dnw_ref[...] = jnp.zeros_like(dnw_ref)

        @pl.when(i <= 1)
        def _():
            dss_ref[...] = jnp.zeros_like(dss_ref)

        dh_ = dh_ref[...]
        nw = nw_ref[...]
        scale = ss_ref[0][1:2]
        dxn, xh = _norm_bwd_rows(t_ref[...], dh_, nw, scale)

        @pl.when(i >= 1)
        def _():
            dx_ref[...] = dx1_ref[...] + dxn

        dsh = jnp.sum(dh_, axis=0, keepdims=True)
        dsc = jnp.sum(dh_ * xh * nw, axis=0, keepdims=True)
        dss_ref[...] += jnp.concatenate([dsh, dsc], axis=0)[None]
        dnw_ref[...] += jnp.sum(dh_ * xh * (1.0 + scale), axis=0, keepdims=True)

    row = pl.BlockSpec((TM, D), lambda i: (i, 0))
    lat = pl.BlockSpec((TM, D), lambda i: (jnp.maximum(i - 1, 0), 0))
    sel = pl.BlockSpec((1, 2, D), lambda i: (jnp.minimum(i, 1), 0, 0))
    return _pcall(body, name="mod1_bwd", grid=(tt // TM,),
                  in_specs=[row, row, lat, _full((1, D)), sel],
                  out_specs=[lat, sel, _full((1, D))],
                  out_shape=[jax.ShapeDtypeStruct((s_len, D), F32), jax.ShapeDtypeStruct((2, 2, D), F32),
                             jax.ShapeDtypeStruct((1, D), F32)])(tok, dh, dx1, nw1, ss1)


def _tri(rev, transpose=False):
    r = lax.broadcasted_iota(jnp.int32, (CH, CH), 0)
    c = lax.broadcasted_iota(jnp.int32, (CH, CH), 1)
    lower = (c >= r) if (rev != transpose) else (c <= r)
    return lower


def _hgrn_gate(fl, qraw, lg):
    lb = 1.0 / (1.0 + jnp.exp(lg[1:2] - lg[0:1]))
    sg = _sig(fl)
    f = lb + (1.0 - lb) * sg
    q = qraw * _sig(qraw) * (HGD ** -0.5)
    return lb, sg, f, q


def _hgrn_fwd(p, lg, *, rev):
    tt = p.shape[0]
    nt = tt // TM
    ncht = TM // CH
    d = 1 if rev else 0

    def tile_of(s):
        return jnp.where(s == 0, 0, nt - s) if rev else s

    def body(f_ref, inp_ref, q_ref, lg_ref, o_ref, st_ref, lf_s, k_s, q_s, state):
        s = pl.program_id(0)

        @pl.when(s == 0)
        def _():
            state[...] = jnp.zeros_like(state)

        _, _, f, q = _hgrn_gate(f_ref[...], q_ref[...], lg_ref[0])
        lf_s[...] = jnp.log(f)
        k_s[...] = 1.0 - f
        q_s[...] = q
        tri = _tri(rev)
        trif = tri.astype(F32)

        def chunk(cc, carry):
            c = (ncht - 1 - cc) if rev else cc
            r0 = pl.multiple_of(c * CH, CH)
            lf = lf_s[pl.ds(r0, CH), :]
            cum = _dot(trif, lf, prec=HI)
            tot = jnp.sum(lf, axis=0, keepdims=True)
            qd = q_s[pl.ds(r0, CH), :] * jnp.exp(cum)
            kk = k_s[pl.ds(r0, CH), :]
            kd = kk * jnp.exp(-cum)
            ke = kk * jnp.exp(tot - cum)
            et = jnp.exp(tot)
            v = inp_ref[pl.ds(r0, CH), :]
            outs = []
            for h in range(4):
                sl = slice(h * HGD, (h + 1) * HGD)
                st0 = state[h]
                st_ref[c, h] = st0
                pm = jnp.where(tri, _bdot(qd[:, sl], kd[:, sl], NT), 0.0)
                outs.append(_bdot(pm, v[:, sl]) + _bdot(qd[:, sl], st0, NT))
                state[h] = st0 * et[:, sl] + _bdot(v[:, sl], ke[:, sl], TN)
            o_ref[pl.ds(r0, CH), :] = jnp.concatenate(outs, axis=1)
            return carry

        lax.fori_loop(0, ncht, chunk, 0)

    def col(cb):
        return pl.BlockSpec((TM, HGW), lambda s: (tile_of(s), cb))

    return _pcall(
        body, name="hgrn_fwd_rev" if rev else "hgrn_fwd", grid=(nt,),
        in_specs=[col(C_FB if rev else C_FF), col(C_INP), col(C_QHG),
                  pl.BlockSpec((1, 2, HGW), lambda s: (d, 0, 0))],
        out_specs=[pl.BlockSpec((TM, HGW), lambda s: (tile_of(s), 0)),
                   pl.BlockSpec((ncht, 4, HGD, HGD), lambda s: (tile_of(s), 0, 0, 0))],
        out_shape=[jax.ShapeDtypeStruct((tt, HGW), F32),
                   jax.ShapeDtypeStruct((nt * ncht, 4, HGD, HGD), F32)],
        scratch=[pltpu.VMEM((TM, HGW), F32)] * 3 + [pltpu.VMEM((4, HGD, HGD), F32)])(p, p, p, lg)


def _hgrn_bwd(p, lg, do, st, dp, prev, *, rev):
    tt = p.shape[0]
    nt = tt // TM
    ncht = TM // CH
    d = 1 if rev else 0
    second = prev is not None

    def tile_of(s):
        return jnp.where(s == nt - 1, 0, s + 1) if rev else nt - 1 - s

    def body(*refs):
        if second:
            (f_ref, inp_ref, q_ref, lg_ref, do_ref, st_ref, dvp_ref, dqp_ref, _dp_in,
             dp_ref, dlg_ref, lf_s, k_s, q_s, do_s, dq_s, dk_s, dv_s, dlf_s, dstate) = refs
        else:
            (f_ref, inp_ref, q_ref, lg_ref, do_ref, st_ref, _dp_in,
             dp_ref, dv_ref, dq_ref, dlg_ref, lf_s, k_s, q_s, do_s, dq_s, dk_s, dv_s, dlf_s, dstate) = refs
        s = pl.program_id(0)
        tile = tile_of(s)

        @pl.when(s == 0)
        def _():
            dstate[...] = jnp.zeros_like(dstate)
            dlg_ref[...] = jnp.zeros_like(dlg_ref)

        qraw = q_ref[...]
        lb, sg, f, q = _hgrn_gate(f_ref[...], qraw, lg_ref[0])
        lf_s[...] = jnp.log(f)
        k_s[...] = 1.0 - f
        q_s[...] = q
        do_s[...] = jnp.where(tile == 0, 0.0, do_ref[...])
        tri = _tri(rev)
        trit = _tri(rev, transpose=True)
        tritf = trit.astype(F32)
        trif = tri.astype(F32)

        def chunk(cc, carry):
            c = cc if rev else (ncht - 1 - cc)
            r0 = pl.multiple_of(c * CH, CH)
            lf = lf_s[pl.ds(r0, CH), :]
            cum = _dot(trif, lf, prec=HI)
            tot = jnp.sum(lf, axis=0, keepdims=True)
            ea = jnp.exp(cum)
            eb = jnp.exp(-cum)
            ee = jnp.exp(tot - cum)
            et = jnp.exp(tot)
            qd = q_s[pl.ds(r0, CH), :] * ea
            kk = k_s[pl.ds(r0, CH), :]
            kd = kk * eb
            ke = kk * ee
            v = inp_ref[pl.ds(r0, CH), :]
            doc = do_s[pl.ds(r0, CH), :]
            dq_l, dk_l, dv_l, dcum_l, dtot_l = [], [], [], [], []
            for h in range(4):
                sl = slice(h * HGD, (h + 1) * HGD)
                qd_, kd_, ke_, v_, do_ = qd[:, sl], kd[:, sl], ke[:, sl], v[:, sl], doc[:, sl]
                st0 = st_ref[c, h]
                ds1 = dstate[h]
                pmt = jnp.where(trit, _bdot(kd_, qd_, NT), 0.0)
                dpm = jnp.where(tri, _bdot(do_, v_, NT), 0.0)
                dpmt = jnp.where(trit, _bdot(v_, do_, NT), 0.0)
                dv = _bdot(pmt, do_) + _bdot(ke_, ds1, NT)
                dqd = _bdot(dpm, kd_) + _bdot(do_, st0)
                dkd = _bdot(dpmt, qd_)
                dke = _bdot(v_, ds1)
                dstate[h] = ds1 * et[:, sl] + _bdot(do_, qd_, TN)
                dtot_l.append(jnp.sum(ds1 * st0, axis=0, keepdims=True) * et[:, sl]
                              + jnp.sum(dke * ke_, axis=0, keepdims=True))
                dq_l.append(dqd * ea[:, sl])
                dk_l.append(dkd * eb[:, sl] + dke * ee[:, sl])
                dv_l.append(dv)
                dcum_l.append(dqd * qd_ - dkd * kd_ - dke * ke_)
            dcum = jnp.concatenate(dcum_l, axis=1)
            dlf = _dot(tritf, dcum, prec=HI) + jnp.concatenate(dtot_l, axis=1)
            dq_s[pl.ds(r0, CH), :] = jnp.concatenate(dq_l, axis=1)
            dk_s[pl.ds(r0, CH), :] = jnp.concatenate(dk_l, axis=1)
            dv_s[pl.ds(r0, CH), :] = jnp.concatenate(dv_l, axis=1)
            dlf_s[pl.ds(r0, CH), :] = dlf
            return carry

        lax.fori_loop(0, ncht, chunk, 0)

        df = dlf_s[...] / f - dk_s[...]
        dfl = df * (1.0 - lb) * sg * (1.0 - sg)
        dlb = jnp.sum(df * (1.0 - sg), axis=0, keepdims=True)
        dl0 = dlb * lb * (1.0 - lb)
        dlg_ref[...] += jnp.concatenate([dl0, -dl0], axis=0)[None]
        if second:
            sq = _sig(qraw)
            dqr = (dqp_ref[...] + dq_s[...]) * (HGD ** -0.5) * (sq * (1.0 + qraw * (1.0 - sq)))
            dp_ref[...] = jnp.concatenate([dfl, dvp_ref[...] + dv_s[...], dqr], axis=1).astype(BF16)
        else:
            dp_ref[...] = dfl.astype(BF16)
            dv_ref[...] = dv_s[...]
            dq_ref[...] = dq_s[...]

    def col(cb):
        return pl.BlockSpec((TM, HGW), lambda s: (tile_of(s), cb))

    tok = pl.BlockSpec((TM, HGW), lambda s: (tile_of(s), 0))
    in_specs = [col(C_FB if rev else C_FF), col(C_INP), col(C_QHG),
                pl.BlockSpec((1, 2, HGW), lambda s: (d, 0, 0)),
                pl.BlockSpec((TM, HGW), lambda s: (jnp.maximum(tile_of(s) - 1, 0), 0)),
                pl.BlockSpec((ncht, 4, HGD, HGD), lambda s: (tile_of(s), 0, 0, 0))]
    args = [p, p, p, lg, do, st]
    dlg_spec = _full((1, 2, HGW))
    dlg_shape = jax.ShapeDtypeStruct((1, 2, HGW), F32)
    if second:
        in_specs += [tok, tok]
        args += [prev[0], prev[1]]
        out_specs = [pl.BlockSpec((TM, 3 * HGW), lambda s: (tile_of(s), 0)), dlg_spec]
        out_shape = [jax.ShapeDtypeStruct(dp.shape, BF16), dlg_shape]
    else:
        out_specs = [pl.BlockSpec((TM, HGW), lambda s: (tile_of(s), C_FB if rev else C_FF)), tok, tok, dlg_spec]
        out_shape = [jax.ShapeDtypeStruct(dp.shape, BF16), jax.ShapeDtypeStruct((tt, HGW), F32),
                     jax.ShapeDtypeStruct((tt, HGW), F32), dlg_shape]
    in_specs.append(ANY)
    args.append(dp)
    return _pcall(body, name="hgrn_bwd_rev" if rev else "hgrn_bwd", grid=(nt,),
                  in_specs=in_specs, out_specs=out_specs, out_shape=out_shape,
                  scratch=[pltpu.VMEM((TM, HGW), F32)] * 8 + [pltpu.VMEM((4, HGD, HGD), F32)],
                  aliases={len(args) - 1: 0})(*args)


def _head_rms(o, w, nheads):
    outs = []
    for h in range(nheads):
        oh = o[:, h * HGD:(h + 1) * HGD]
        outs.append(oh * lax.rsqrt(jnp.mean(oh * oh, axis=-1, keepdims=True) + EPS))
    return jnp.concatenate(outs, axis=1)


def _readout(o0, o1, p, hw4):
    s_len = o0.shape[0] - L

    def body(o0_ref, o1_ref, g_ref, w_ref, y_ref):
        xh = _head_rms(o0_ref[...] + o1_ref[...], None, 4)
        g = g_ref[...]
        y_ref[...] = (xh * w_ref[...] * (g * _sig(g))).astype(BF16)

    lat = pl.BlockSpec((TM, HGW), lambda i: (i + 1, 0))
    return _pcall(body, name="readout", grid=(s_len // TM,),
                  in_specs=[lat, lat, pl.BlockSpec((TM, HGW), lambda i: (i + 1, C_GHG)), _full((1, HGW))],
                  out_specs=pl.BlockSpec((TM, HGW), lambda i: (i, 0)),
                  out_shape=jax.ShapeDtypeStruct((s_len, HGW), BF16))(o0, o1, p, hw4)


def _readout_bwd(o0, o1, p, hw4, dy, dp):
    tt = o0.shape[0]
    s_len = tt - L

    def body(o0_ref, o1_ref, g_ref, w_ref, dy_ref, _dp_in, dp_ref, do_ref, dw_ref):
        i = pl.program_id(0)

        @pl.when(i == 0)
        def _():
            dw_ref[...] = jnp.zeros_like(dw_ref)
            dp_ref[...] = jnp.zeros_like(dp_ref)

        @pl.when(i >= 1)
        def _():
            o = o0_ref[...] + o1_ref[...]
            g = g_ref[...]
            w = w_ref[...]
            sg = _sig(g)
            dy_ = dy_ref[...]
            dsw = dy_ * (g * sg)
            outs, xhs = [], []
            for h in range(4):
                sl = slice(h * HGD, (h + 1) * HGD)
                oh = o[:, sl]
                r = lax.rsqrt(jnp.mean(oh * oh, axis=-1, keepdims=True) + EPS)
                xh = oh * r
                dxh = dsw[:, sl] * w[:, sl]
                outs.append(r * (dxh - xh * jnp.mean(dxh * xh, axis=-1, keepdims=True)))
                xhs.append(xh)
            xh = jnp.concatenate(xhs, axis=1)
            do_ref[...] = jnp.concatenate(outs, axis=1)
            dp_ref[...] = (dy_ * xh * w * (sg * (1.0 + g * (1.0 - sg)))).astype(BF16)
            dw_ref[...] += jnp.sum(dsw * xh, axis=0, keepdims=True)

    tok = pl.BlockSpec((TM, HGW), lambda i: (i, 0))
    lat = pl.BlockSpec((TM, HGW), lambda i: (jnp.maximum(i - 1, 0), 0))
    return _pcall(body, name="readout_bwd", grid=(tt // TM,),
                  in_specs=[tok, tok, pl.BlockSpec((TM, HGW), lambda i: (i, C_GHG)), _full((1, HGW)), lat, ANY],
                  out_specs=[pl.BlockSpec((TM, HGW), lambda i: (i, C_GHG)), lat, _full((1, HGW))],
                  out_shape=[jax.ShapeDtypeStruct(dp.shape, BF16), jax.ShapeDtypeStruct((s_len, HGW), F32),
                             jax.ShapeDtypeStruct((1, HGW), F32)],
                  aliases={5: 0})(o0, o1, p, hw4, dy, dp)


def _rope_tables(s_len):
    t = np.arange(s_len)
    inv = ROPE_THETA ** (-np.arange(0, 32, 2, dtype=np.float64) / 32)
    def half(pos):
        ang = pos[:, None].astype(np.float64) * inv[None, :]
        return (np.concatenate([np.cos(ang), np.cos(ang)], 1), np.concatenate([-np.sin(ang), np.sin(ang)], 1))
    cr, sr = half(t // GRID_W)
    cc, sc = half(t % GRID_W)
    cos = np.concatenate([cr, cc, cr, cc], 1)
    sin = np.concatenate([sr, sc, sr, sc], 1)
    cos = np.concatenate([np.ones((L, 128)), cos], 0)
    sin = np.concatenate([np.zeros((L, 128)), sin], 0)
    return jnp.asarray(cos, F32), jnp.asarray(sin, F32)


def _blockdiag(n, w):
    i = np.arange(n)
    return jnp.asarray((i[:, None] // w == i[None, :] // w) / float(w), F32)


def _dup_matrix():
    m = np.zeros((128, 512), np.float32)
    for g in range(2):
        for j in range(4):
            for dd in range(HDIM):
                m[64 * g + dd, 256 * g + 64 * j + dd] = 1.0
    return m


def _rot(x):
    n = x.shape[1]
    lane = lax.broadcasted_iota(jnp.int32, x.shape, 1)
    return jnp.where((lane % 32) < 16, pltpu.roll(x, n - 16, 1), pltpu.roll(x, 16, 1))


def _qk_prep(p, cos, sin, qnw8, knw2, bd512, bd128, dup):
    tt = p.shape[0]

    def body(q_ref, kv_ref, cos_ref, sin_ref, qw_ref, kw_ref, b5_ref, b1_ref, dup_ref,
             qr_ref, k4_ref, v4_ref):
        cos_, sin_ = cos_ref[...], sin_ref[...]
        q = q_ref[...]
        qn = q * lax.rsqrt(_dot(q * q, b5_ref[...], prec=HI) + EPS) * qw_ref[...]
        cos4 = jnp.concatenate([cos_] * 4, axis=1)
        sin4 = jnp.concatenate([sin_] * 4, axis=1)
        qr_ref[...] = ((qn * cos4 + _rot(qn) * sin4) * (HDIM ** -0.5)).astype(BF16)
        kv = kv_ref[...]
        k, v = kv[:, :128], kv[:, 128:]
        kn = k * lax.rsqrt(_dot(k * k, b1_ref[...], prec=HI) + EPS) * kw_ref[...]
        kr = kn * cos_ + _rot(kn) * sin_
        k4_ref[...] = _bdot(kr, dup_ref[...]).astype(BF16)
        v4_ref[...] = _bdot(v, dup_ref[...]).astype(BF16)

    row = lambda w, cb: pl.BlockSpec((TM, w), lambda i: (i, cb))
    out = jax.ShapeDtypeStruct((tt, ATW), BF16)
    return _pcall(body, name="qk_prep", grid=(tt // TM,),
                  in_specs=[row(ATW, C_QRAW), row(256, C_KV), row(128, 0), row(128, 0),
                            _full((1, ATW)), _full((1, 128)), _full((ATW, ATW)), _full((128, 128)),
                            _full((128, ATW))],
                  out_specs=[row(ATW, 0)] * 3, out_shape=[out] * 3)(
                      p, p, cos, sin, qnw8, knw2, bd512, bd128, dup)


def _attn_masks(i, nb):
    r = lax.broadcasted_iota(jnp.int32, (BLK, 3 * BLK + L), 0)
    c = lax.broadcasted_iota(jnp.int32, (BLK, 3 * BLK + L), 1)
    kpos = (i - 1) * BLK + c
    loc = (jnp.abs(c - BLK - r) <= BLK) & (kpos >= 0) & (kpos < nb * BLK)
    return loc | (c >= 3 * BLK)


def _lane_mask(j):
    lane = lax.broadcasted_iota(jnp.int32, (1, 256), 1)
    return (lane // HDIM) == j


def _attn_specs(nb):
    blk = lambda off: pl.BlockSpec((BLK, ATW), lambda i: (jnp.clip(i + off, 0, nb - 1) + 2, 0))
    ctx = pl.BlockSpec((L, ATW), lambda i: (0, 0))
    return blk, ctx


def _attn_fwd(qr, k4, v4, sinks):
    tt = qr.shape[0]
    s_len = tt - L
    nb = s_len // BLK

    def body(sk_ref, q_ref, kp, ko, kn, kc, vp, vo, vn, vc, y_ref, lse_ref):
        i = pl.program_id(0)
        valid = _attn_masks(i, nb)
        q = q_ref[...]
        ys, lses = [], []
        for g in range(2):
            gs = slice(256 * g, 256 * g + 256)
            kcat = jnp.concatenate([kp[:, gs], ko[:, gs], kn[:, gs], kc[:, gs]], axis=0)
            vcat = jnp.concatenate([vp[:, gs], vo[:, gs], vn[:, gs], vc[:, gs]], axis=0)
            qg = q[:, gs]
            og = jnp.zeros((BLK, 256), F32)
            lg = jnp.zeros((BLK, 256), F32)
            for j in range(4):
                lm = _lane_mask(j)
                sink = sk_ref[4 * g + j]
                s = jnp.where(valid, _dot(jnp.where(lm, qg, jnp.zeros_like(qg)), kcat, NT), -1e30)
                m = jnp.maximum(jnp.max(s, axis=-1, keepdims=True), sink)
                e = jnp.exp(s - m)
                den = jnp.sum(e, axis=-1, keepdims=True) + jnp.exp(sink - m)
                pr = e / den
                og = og + jnp.where(lm, _bdot(pr, vcat), 0.0)
                lg = lg + jnp.where(lm, m + jnp.log(den), 0.0)
            ys.append(og)
            lses.append(lg)
        y_ref[...] = jnp.concatenate(ys, axis=1).astype(BF16)
        lse_ref[...] = jnp.concatenate(lses, axis=1)

    blk, ctx = _attn_specs(nb)
    out = pl.BlockSpec((BLK, ATW), lambda i: (i, 0))
    return _pcall(body, name="attn_fwd", grid=(nb,),
                  in_specs=[pl.BlockSpec(memory_space=pltpu.SMEM), blk(0),
                            blk(-1), blk(0), blk(1), ctx, blk(-1), blk(0), blk(1), ctx],
                  out_specs=[out, out],
                  out_shape=[jax.ShapeDtypeStruct((s_len, ATW), BF16),
                             jax.ShapeDtypeStruct((s_len, ATW), F32)])(
                      sinks, qr, k4, k4, k4, k4, v4, v4, v4, v4)


def _attn_bwd(qr, k4, v4, sinks, y, lse, dy):
    tt = qr.shape[0]
    s_len = tt - L
    nb = s_len // BLK

    def body(sk_ref, q_ref, kp, ko, kn, kc, vp, vo, vn, vc, y_ref, lse_ref, dy_ref,
             dq_ref, dkw_ref, dvw_ref, dkc_ref, dvc_ref, dsk_ref):
        i = pl.program_id(0)

        @pl.when(i == 0)
        def _():
            dkc_ref[...] = jnp.zeros_like(dkc_ref)
            dvc_ref[...] = jnp.zeros_like(dvc_ref)
            dsk_ref[...] = jnp.zeros_like(dsk_ref)

        valid = _attn_masks(i, nb)
        q = q_ref[...]
        dy_ = dy_ref[...]
        dly = dy_ * y_ref[...].astype(F32)
        lse_ = lse_ref[...]
        dqs = []
        for g in range(2):
            gs = slice(256 * g, 256 * g + 256)
            kcat = jnp.concatenate([kp[:, gs], ko[:, gs], kn[:, gs], kc[:, gs]], axis=0)
            vcat = jnp.concatenate([vp[:, gs], vo[:, gs], vn[:, gs], vc[:, gs]], axis=0)
            qg, dyg, dlg, lsg = q[:, gs], dy_[:, gs], dly[:, gs], lse_[:, gs]
            dqg = jnp.zeros((BLK, 256), F32)
            dkg = jnp.zeros((3 * BLK + L, 256), F32)
            dvg = jnp.zeros((3 * BLK + L, 256), F32)
            for j in range(4):
                lm = _lane_mask(j)
                sink = sk_ref[4 * g + j]
                qm = jnp.where(lm, qg, jnp.zeros_like(qg))
                dym = jnp.where(lm, dyg, 0.0).astype(BF16)
                lse_h = jnp.max(jnp.where(lm, lsg, -1e30), axis=-1, keepdims=True)
                delta = jnp.sum(jnp.where(lm, dlg, 0.0), axis=-1, keepdims=True)
                s = _dot(qm, kcat, NT)
                pr = jnp.where(valid, jnp.exp(s - lse_h), 0.0)
                dpr = _dot(dym, vcat, NT)
                dsc = pr * (dpr - delta)
                psink = jnp.exp(sink - lse_h)
                dsk_ref[4 * g + j:4 * g + j + 1, :] += jnp.broadcast_to(
                    -jnp.sum(psink * delta, axis=0, keepdims=True), (1, 128))
                dsb = dsc.astype(BF16)
                dqg = dqg + jnp.where(lm, _dot(dsb, kcat), 0.0)
                dkg = dkg + _dot(dsb, qm, TN)
                dvg = dvg + _dot(pr.astype(BF16), dym, TN)
            dqs.append(dqg)
            dkw_ref[0, :, gs] = dkg[:3 * BLK]
            dvw_ref[0, :, gs] = dvg[:3 * BLK]
            dkc_ref[:, gs] += dkg[3 * BLK:]
            dvc_ref[:, gs] += dvg[3 * BLK:]
        dq_ref[...] = jnp.concatenate(dqs, axis=1)

    blk, ctx = _attn_specs(nb)
    out = pl.BlockSpec((BLK, ATW), lambda i: (i, 0))
    win = pl.BlockSpec((1, 3 * BLK, ATW), lambda i: (i, 0, 0))
    acc = _full((L, ATW))
    return _pcall(body, name="attn_bwd", grid=(nb,),
                  in_specs=[pl.BlockSpec(memory_space=pltpu.SMEM), blk(0),
                            blk(-1), blk(0), blk(1), ctx, blk(-1), blk(0), blk(1), ctx, out, out, out],
                  out_specs=[out, win, win, acc, acc, _full((8, 128))],
                  out_shape=[jax.ShapeDtypeStruct((s_len, ATW), F32),
                             jax.ShapeDtypeStruct((nb, 3 * BLK, ATW), F32),
                             jax.ShapeDtypeStruct((nb, 3 * BLK, ATW), F32),
                             jax.ShapeDtypeStruct((L, ATW), F32), jax.ShapeDtypeStruct((L, ATW), F32),
                             jax.ShapeDtypeStruct((8, 128), F32)])(
                      sinks, qr, k4, k4, k4, k4, v4, v4, v4, v4, y, lse, dy)


def _attn_post(p, cos, sin, qnw8, knw2, bd512, bd128, dupt, dq, dkw, dvw, dkc, dvc, dp):
    tt = p.shape[0]
    s_len = tt - L
    nb = s_len // BLK
    nctx = L // BLK

    def body(q_ref, kv_ref, cos_ref, sin_ref, qw_ref, kw_ref, b5_ref, b1_ref, dupt_ref,
             dq_ref, kwp, kwo, kwn, vwp, vwo, vwn, dkc_ref, dvc_ref, _dp_in,
             dp_ref, dqw_ref, dkw_ref):
        t = pl.program_id(0)
        j = t - nctx

        @pl.when(t == 0)
        def _():
            dqw_ref[...] = jnp.zeros_like(dqw_ref)
            dkw_ref[...] = jnp.zeros_like(dkw_ref)

        is_lat = t >= nctx
        cos_, sin_ = cos_ref[...], sin_ref[...]
        has_p = is_lat & (j >= 1)
        has_n = is_lat & (j <= nb - 2)
        dk4 = (jnp.where(is_lat, kwo[0], dkc_ref[...]) + jnp.where(has_p, kwp[0], 0.0)
               + jnp.where(has_n, kwn[0], 0.0))
        dv4 = (jnp.where(is_lat, vwo[0], dvc_ref[...]) + jnp.where(has_p, vwp[0], 0.0)
               + jnp.where(has_n, vwn[0], 0.0))
        dkr = _dot(dk4, dupt_ref[...], prec=HI)
        dv = _dot(dv4, dupt_ref[...], prec=HI)
        kv = kv_ref[...]
        k = kv[:, :128]
        kw = kw_ref[...]
        rk = lax.rsqrt(_dot(k * k, b1_ref[...], prec=HI) + EPS)
        xk = k * rk
        dkn = dkr * cos_ + _rot(dkr * sin_)
        dxk = dkn * kw
        dk = rk * (dxk - xk * _dot(dxk * xk, b1_ref[...], prec=HI))
        dkw_ref[...] += jnp.sum(dkn * xk, axis=0, keepdims=True)
        q = q_ref[...]
        qw = qw_ref[...]
        rq = lax.rsqrt(_dot(q * q, b5_ref[...], prec=HI) + EPS)
        xq = q * rq
        cos4 = jnp.concatenate([cos_] * 4, axis=1)
        sin4 = jnp.concatenate([sin_] * 4, axis=1)
        dqr = jnp.where(is_lat, dq_ref[...], 0.0) * (HDIM ** -0.5)
        dqn = dqr * cos4 + _rot(dqr * sin4)
        dxq = dqn * qw
        dqraw = rq * (dxq - xq * _dot(dxq * xq, b5_ref[...], prec=HI))
        dqw_ref[...] += jnp.sum(dqn * xq, axis=0, keepdims=True)
        dp_ref[...] = jnp.concatenate([dqraw, dk, dv], axis=1).astype(BF16)

    row = lambda w, cb: pl.BlockSpec((BLK, w), lambda t: (t, cb))
    lat = pl.BlockSpec((BLK, ATW), lambda t: (jnp.maximum(t - nctx, 0), 0))

    def part(off):
        return pl.BlockSpec((1, BLK, ATW), lambda t: (jnp.clip(t - nctx + off, 0, nb - 1), 1 - off, 0))

    cacc = pl.BlockSpec((BLK, ATW), lambda t: (jnp.minimum(t, nctx - 1), 0))
    return _pcall(body, name="attn_post", grid=(tt // BLK,),
                  in_specs=[row(ATW, C_QRAW), row(256, C_KV), row(128, 0), row(128, 0),
                            _full((1, ATW)), _full((1, 128)), _full((ATW, ATW)), _full((128, 128)),
                            _full((ATW, 128)), lat, part(-1), part(0), part(1), part(-1), part(0), part(1),
                            cacc, cacc, ANY],
                  out_specs=[pl.BlockSpec((BLK, 768), lambda t: (t, C_QKV)), _full((1, ATW)), _full((1, 128))],
                  out_shape=[jax.ShapeDtypeStruct(dp.shape, BF16), jax.ShapeDtypeStruct((1, ATW), F32),
                             jax.ShapeDtypeStruct((1, 128), F32)],
                  aliases={18: 0})(p, p, cos, sin, qnw8, knw2, bd512, bd128, dupt,
                                   dq, dkw, dkw, dkw, dvw, dvw, dvw, dkc, dvc, dp)


def _merge(ah, aa, p):
    s_len = ah.shape[0]

    def body(ah_ref, aa_ref, gh_ref, ga_ref, m_ref):
        m_ref[...] = (_sig(gh_ref[...]) * ah_ref[...] + _sig(ga_ref[...]) * aa_ref[...]).astype(BF16)

    row = pl.BlockSpec((TM, D), lambda i: (i, 0))
    return _pcall(body, name="merge", grid=(s_len // TM,),
                  in_specs=[row, row, pl.BlockSpec((TM, D), lambda i: (i + 1, 2)),
                            pl.BlockSpec((TM, D), lambda i: (i + 1, 3))],
                  out_specs=row, out_shape=jax.ShapeDtypeStruct((s_len, D), BF16))(ah, aa, p, p)


def _merge_bwd(dm, ah, aa, p):
    tt = p.shape[0]
    s_len = tt - L

    def body(dm_ref, ah_ref, aa_ref, gh_ref, ga_ref, dp_ref, dmh_ref, dma_ref):
        i = pl.program_id(0)

        @pl.when(i == 0)
        def _():
            dp_ref[...] = jnp.zeros_like(dp_ref)

        @pl.when(i >= 1)
        def _():
            dm_ = dm_ref[...]
            sh, sa = _sig(gh_ref[...]), _sig(ga_ref[...])
            dp_ref[...] = jnp.concatenate([dm_ * ah_ref[...] * sh * (1.0 - sh),
                                           dm_ * aa_ref[...] * sa * (1.0 - sa)], axis=1).astype(BF16)
            dmh_ref[...] = (dm_ * sh).astype(BF16)
            dma_ref[...] = (dm_ * sa).astype(BF16)

    lat = pl.BlockSpec((TM, D), lambda i: (jnp.maximum(i - 1, 0), 0))
    return _pcall(body, name="merge_bwd", grid=(tt // TM,),
                  in_specs=[lat, lat, lat, pl.BlockSpec((TM, D), lambda i: (i, 2)),
                            pl.BlockSpec((TM, D), lambda i: (i, 3))],
                  out_specs=[pl.BlockSpec((TM, 2 * D), lambda i: (i, C_GATES)), lat, lat],
                  out_shape=[jax.ShapeDtypeStruct((tt, NCOL), BF16), jax.ShapeDtypeStruct((s_len, D), BF16),
                             jax.ShapeDtypeStruct((s_len, D), BF16)])(dm, ah, aa, p, p)


def _local_step(x, ctx, tgt, mod, modc, nw1, nw2, lg, hw, qnw, knw, sinks,
                w_in, bh4, ba4, w_o, g4, u4, dn4):
    s_len = x.shape[0]
    tt = s_len + L
    tok = jnp.concatenate([ctx, x], axis=0)
    ss1 = jnp.stack([modc, mod[0:2]])
    ss2 = mod[3:5][None]
    g1, g2 = mod[2:3], mod[5:6]
    hw4 = jnp.tile(hw, (1, 4))
    qnw8 = jnp.tile(qnw, (1, 8))
    knw2 = jnp.tile(knw, (1, 2))
    cos, sin = _rope_tables(s_len)
    bd512, bd128 = _blockdiag(ATW, HDIM), _blockdiag(128, HDIM)
    dupm = _dup_matrix()
    dup, dupt = jnp.asarray(dupm, BF16), jnp.asarray(dupm.T, F32)
    tmt = tt // 2 if tt % 512 else 512

    h = _modulate(tok, nw1, ss1, name="mod1", sel=lambda i: jnp.minimum(i, 1))
    p = _mm(h, w_in, name="mm_in", tm=tmt, tn=768, tk=D)
    o0, st0 = _hgrn_fwd(p, lg, rev=False)
    o1, st1 = _hgrn_fwd(p, lg, rev=True)
    y_hg = _readout(o0, o1, p, hw4)
    qr, k4, v4 = _qk_prep(p, cos, sin, qnw8, knw2, bd512, bd128, dup)
    y_at, lse = _attn_fwd(qr, k4, v4, sinks)
    ah = _mm_cs(y_hg, bh4, name="mm_bh")
    aa = _mm_cs(y_at, ba4, name="mm_ba")
    mixed = _merge(ah, aa, p)
    ao = _mm(mixed, w_o, name="mm_o", tm=512, tn=D, tk=D)
    x1, h2 = _res1_mod2(x, ao, g1, nw2, ss2)
    a4, b4, z4 = _ffn_up(h2, g4, u4)
    y = _ffn_down(z4, dn4)
    sq, dx2, dyb, dg2 = _loss_head(x1, y, g2, tgt)

    da4, db4 = _ffn_dz(dyb, dn4, a4, b4)
    g_dn = _ffn_gdn(z4, dyb)
    dh2 = _ffn_dh2(da4, db4, g4, u4)
    g_g, g_u = _ffn_ggu(h2, da4, db4)
    dx1, dattn, dss2, dnw2, dg1 = _mod2_bwd(x1, dh2, dx2, ao, nw2, ss2, g1)
    dm = _mm(dattn, w_o, name="mm_dm", mode="nt", tm=512, tn=D, tk=D)
    g_o = _mm(mixed, dattn, name="mm_go", mode="tn", tm=D, tn=D, tk=512)
    dp, dmh, dma = _merge_bwd(dm, ah, aa, p)
    dy_hg = _mm_cs_nt(dmh, bh4, name="mm_dyh")
    dy_at = _mm_cs_nt(dma, ba4, name="mm_dya")
    g_bh = _mm_cs_tn(y_hg, dmh, D // 4, name="mm_gbh")
    g_ba = _mm_cs_tn(y_at, dma, D // 4, name="mm_gba")
    dp, do, dhw4 = _readout_bwd(o0, o1, p, hw4, dy_hg, dp)
    dq, dkw, dvw, dkc, dvc, dsk = _attn_bwd(qr, k4, v4, sinks, y_at, lse, dy_at)
    dp, dqnw8, dknw2 = _attn_post(p, cos, sin, qnw8, knw2, bd512, bd128, dupt, dq, dkw, dvw, dkc, dvc, dp)
    dp, dv0, dq0, dlg0 = _hgrn_bwd(p, lg, do, st0, dp, None, rev=False)
    dp, dlg1 = _hgrn_bwd(p, lg, do, st1, dp, (dv0, dq0), rev=True)
    dh = _mm(dp, w_in, name="mm_dh", mode="nt", tm=tmt, tn=D, tk=768)
    g_in = _mm(h, dp, name="mm_gin", mode="tn", tm=D, tn=768, tk=tmt)
    gx, dss1, dnw1 = _mod1_bwd(tok, dh, dx1, nw1, ss1)

    dmod = jnp.concatenate([dss1[1], dg1, dss2, dg2], axis=0)
    dmodc = dss1[0]
    raw = (dss1, dg1, dss2, dg2, dnw1, dnw2, dhw4, dqnw8, dknw2, dsk, dlg0, dlg1)
    small = dict(raw=raw, dmod=dmod, dmodc=dmodc, dnw1=dnw1, dnw2=dnw2,
                 dhw=dhw4.reshape(4, HGD).sum(0, keepdims=True),
                 dqnw=dqnw8.reshape(8, HDIM).sum(0, keepdims=True),
                 dknw=dknw2.reshape(2, HDIM).sum(0, keepdims=True),
                 dsinks=dsk[:, 0], dlg=jnp.concatenate([dlg0, dlg1], axis=0))
    big = dict(w_in=g_in, w_bh=g_bh, w_ba=g_ba, w_o=g_o, w_g=g_g, w_u=g_u, w_dn=g_dn)
    return sq, gx, big, small


def _to_kernel_cols(w):
    return jnp.concatenate([w[..., 512:1024], w[..., 1024:1536], w[..., 1792:2304], w[..., 0:512],
                            w[..., 3328:5376], w[..., 2304:2816], w[..., 2816:3328], w[..., 1536:1792]], axis=-1)


def _from_kernel_cols(g):
    return jnp.concatenate([g[..., 1536:2048], g[..., 0:512], g[..., 512:1024], g[..., 5120:5376],
                            g[..., 1024:1536], g[..., 4096:4608], g[..., 4608:5120], g[..., 2048:4096]], axis=-1)


def _place():
    x, y, c = lax.axis_index("x"), lax.axis_index("y"), lax.axis_index("c")
    return x, y, c


def _gather_blocks(x_refs, out_refs, send_sems, recv_sems, local_sems):
    n = len(x_refs)
    x, y, c = _place()
    me, sibling = (x, y, c), (x, y, 1 - c)
    chips = [(1 - x, y), (x, 1 - y), (1 - x, 1 - y)]

    def slot(u, px, py, pc):
        return out_refs[u].at[4 * px + 2 * py + pc]

    def copy(u, k, block, to, src=None):
        return pltpu.make_async_remote_copy(
            src_ref=slot(u, *block) if src is None else src, dst_ref=slot(u, *block),
            send_sem=send_sems.at[u, k], recv_sem=recv_sems.at[u, k], device_id=to, device_id_type=MESH)

    mines = [pltpu.make_async_copy(x_refs[u], slot(u, *me), local_sems.at[u]) for u in range(n)]
    for cp in mines:
        cp.start()
    first = []
    for u in range(n):
        first.append(copy(u, 0, me, sibling, src=x_refs[u]))
        first += [copy(u, 1 + j, me, (*chip, c), src=x_refs[u]) for j, chip in enumerate(chips)]
    for cp in first:
        cp.start()
    passed = []
    for j, chip in enumerate(chips):
        for u in range(n):
            copy(u, 1 + j, (*chip, c), me).wait_recv()
            fwd = copy(u, 4 + j, (*chip, c), sibling)
            fwd.start()
            passed.append(fwd)
    for u in range(n):
        copy(u, 0, sibling, me).wait_recv()
    for j, chip in enumerate(chips):
        for u in range(n):
            copy(u, 4 + j, (*chip, 1 - c), me).wait_recv()
    for cp in first + passed:
        cp.wait_send()
    for cp in mines:
        cp.wait()


def _gather_sems(n):
    return [pltpu.SemaphoreType.DMA((n, 7)), pltpu.SemaphoreType.DMA((n, 7)), pltpu.SemaphoreType.DMA((n,))]


def _allgather(blks, *, name, in_vmem):
    n = len(blks)
    space = pltpu.VMEM if in_vmem else pl.ANY

    def body(*refs):
        _gather_blocks(refs[:n], refs[n:2 * n], *refs[2 * n:])

    return pl.pallas_call(
        body, name=name, out_shape=[jax.ShapeDtypeStruct((8,) + b.shape, b.dtype) for b in blks],
        in_specs=[pl.BlockSpec(memory_space=space)] * n, out_specs=[pl.BlockSpec(memory_space=space)] * n,
        scratch_shapes=_gather_sems(n))(*blks)


def _ag_small(raw):
    def body(dss1, dg1, dss2, dg2, dnw1, dnw2, dhw4, dqnw8, dknw2, dsk, dlg0, dlg1,
             out_ref, tot_ref, blk, send_sems, recv_sems, local_sems):
        blk[...] = jnp.zeros_like(blk)
        blk[0:2, :] = dss1[1]
        blk[2:3, :] = dg1[...]
        blk[3:5, :] = dss2[...]
        blk[5:6, :] = dg2[...]
        blk[6:8, :] = dss1[0]
        blk[8:9, :] = dnw1[...]
        blk[9:10, :] = dnw2[...]
        blk[10:11, 0:HGW] = dhw4[...]
        blk[10:11, HGW:D] = dqnw8[...]
        blk[11:12, 0:128] = dknw2[...]
        blk[12:14, 0:HGW] = dlg0[0]
        blk[14:16, 0:HGW] = dlg1[0]
        blk[16:24, 0:128] = dsk[...]
        _gather_blocks([blk], [out_ref], send_sems, recv_sems, local_sems)
        acc = out_ref[0]
        for i in range(1, 8):
            acc = acc + out_ref[i]
        tot_ref[...] = acc

    vm = pl.BlockSpec(memory_space=pltpu.VMEM)
    return pl.pallas_call(
        body, name="ag_small",
        out_shape=[jax.ShapeDtypeStruct((8, 24, D), F32), jax.ShapeDtypeStruct((24, D), F32)],
        in_specs=[vm] * 12, out_specs=[vm, vm],
        scratch_shapes=[pltpu.VMEM((24, D), F32)] + _gather_sems(1))(*raw)


def _rs_pair_exchange(units):
    n = len(units)

    def body(*refs):
        g_refs, r_refs = refs[:n], refs[n:2 * n]
        send_sems, recv_sems = refs[2 * n:]
        x, y, c = _place()
        cps = [pltpu.make_async_remote_copy(
            src_ref=g_refs[u].at[j, 1 - c], dst_ref=r_refs[u].at[j], send_sem=send_sems.at[u, j],
            recv_sem=recv_sems.at[u, j], device_id=(x, y, 1 - c), device_id_type=MESH)
            for u in range(n) for j in range(4)]
        for cp in cps:
            cp.start()
        for cp in cps:
            cp.wait()

    return pl.pallas_call(
        body, name="rs_pair_exchange",
        out_shape=[jax.ShapeDtypeStruct((4,) + g.shape[2:], g.dtype) for g in units],
        in_specs=[ANY] * n, out_specs=[ANY] * n,
        scratch_shapes=[pltpu.SemaphoreType.DMA((n, 4)), pltpu.SemaphoreType.DMA((n, 4))])(*units)


def _rs_pair_add(units, recvs, c):
    n = len(units)

    def body(c_ref, *refs):
        for u in range(n):
            refs[2 * n + u][...] = (refs[u][0] + refs[n + u][...]).astype(BF16)

    in_specs, out_specs, out_shape = [], [], []
    for g in units:
        h, w = g.shape[2] // 2, g.shape[3]
        in_specs.append(pl.BlockSpec((1, 1, h, w), lambda j, i, cr: (j, cr[0], i, 0)))
    for g in units:
        h, w = g.shape[2] // 2, g.shape[3]
        in_specs.append(pl.BlockSpec((1, h, w), lambda j, i, cr: (j, i, 0)))
        out_specs.append(pl.BlockSpec((1, h, w), lambda j, i, cr: (j, i, 0)))
        out_shape.append(jax.ShapeDtypeStruct((4, 2 * h, w), BF16))
    return pl.pallas_call(
        body, name="rs_pair_add",
        grid_spec=pltpu.PrefetchScalarGridSpec(num_scalar_prefetch=1, grid=(4, 2), in_specs=in_specs,
                                               out_specs=out_specs),
        out_shape=out_shape,
        compiler_params=pltpu.CompilerParams(vmem_limit_bytes=48 << 20))(c.reshape(1), *units, *recvs)


def _rs_chip_exchange(pairs):
    n = len(pairs)

    def body(*refs):
        p_refs, r_refs = refs[:n], refs[n:2 * n]
        send_sems, recv_sems, local_sems = refs[2 * n:]
        x, y, c = _place()
        k = 2 * x + y
        locals_ = [pltpu.make_async_copy(p_refs[u].at[k], r_refs[u].at[k], local_sems.at[u]) for u in range(n)]
        for cp in locals_:
            cp.start()
        sends = []
        for d in range(1, 4):
            j = (k + d) % 4
            for u in range(n):
                sends.append(pltpu.make_async_remote_copy(
                    src_ref=p_refs[u].at[j], dst_ref=r_refs[u].at[k], send_sem=send_sems.at[u, d - 1],
                    recv_sem=recv_sems.at[u, d - 1], device_id=(j // 2, j % 2, c), device_id_type=MESH))
        for cp in sends:
            cp.start()
        for d in range(1, 4):
            src = (k + 4 - d) % 4
            for u in range(n):
                pltpu.make_async_remote_copy(
                    src_ref=p_refs[u].at[src], dst_ref=r_refs[u].at[src], send_sem=send_sems.at[u, d - 1],
                    recv_sem=recv_sems.at[u, d - 1], device_id=(x, y, c), device_id_type=MESH).wait_recv()
        for cp in sends:
            cp.wait_send()
        for cp in locals_:
            cp.wait()

    return pl.pallas_call(
        body, name="rs_chip_exchange", out_shape=[jax.ShapeDtypeStruct(p.shape, p.dtype) for p in pairs],
        in_specs=[ANY] * n, out_specs=[ANY] * n,
        scratch_shapes=[pltpu.SemaphoreType.DMA((n, 3)), pltpu.SemaphoreType.DMA((n, 3)),
                        pltpu.SemaphoreType.DMA((n,))])(*pairs)


def _rs_chip_add(contribs):
    n = len(contribs)

    def body(*refs):
        for u in range(n):
            a, b, c_, d = refs[4 * u:4 * u + 4]
            refs[4 * n + u][...] = ((a[0].astype(F32) + b[0].astype(F32)) + c_[0].astype(F32)) + d[0].astype(F32)

    in_specs, out_specs, out_shape, args = [], [], [], []
    for r in contribs:
        h, w = r.shape[1] // 2, r.shape[2]
        in_specs += [pl.BlockSpec((1, h, w), functools.partial(lambda j, i: (j, i, 0), j)) for j in range(4)]
        args += [r] * 4
        out_specs.append(pl.BlockSpec((h, w), lambda i: (i, 0)))
        out_shape.append(jax.ShapeDtypeStruct((2 * h, w), F32))
    return _pcall(body, name="rs_chip_add", grid=(2,), in_specs=in_specs, out_specs=out_specs,
                  out_shape=out_shape)(*args)


def _rs_sibling_gather(reds):
    n = len(reds)

    def body(*refs):
        a_refs, o_refs = refs[:n], refs[n:2 * n]
        send_sems, recv_sems, local_sems = refs[2 * n:]
        x, y, c = _place()
        locals_ = [pltpu.make_async_copy(a_refs[u], o_refs[u].at[c], local_sems.at[u]) for u in range(n)]
        cps = [pltpu.make_async_remote_copy(
            src_ref=a_refs[u], dst_ref=o_refs[u].at[c], send_sem=send_sems.at[u], recv_sem=recv_sems.at[u],
            device_id=(x, y, 1 - c), device_id_type=MESH) for u in range(n)]
        for cp in locals_ + cps:
            cp.start()
        for cp in cps + locals_:
            cp.wait()

    return pl.pallas_call(
        body, name="rs_sibling_gather",
        out_shape=[jax.ShapeDtypeStruct((2,) + r.shape, r.dtype) for r in reds],
        in_specs=[ANY] * n, out_specs=[ANY] * n,
        scratch_shapes=[pltpu.SemaphoreType.DMA((n,))] * 3)(*reds)


def _ada_fwd(c16, w, b):
    n = w.shape[1]
    tn = 512

    def body(c_ref, w_ref, b_ref, o_ref):
        cc = c_ref[...]
        o_ref[...] = _dot(cc * _sig(cc), w_ref[...], prec=HI) + b_ref[...]

    return _pcall(body, name="ada_fwd", grid=(n // tn,),
                  in_specs=[_full((16, D)), pl.BlockSpec((D, tn), lambda j: (0, j)),
                            pl.BlockSpec((1, tn), lambda j: (0, j))],
                  out_specs=pl.BlockSpec((16, tn), lambda j: (0, j)),
                  out_shape=jax.ShapeDtypeStruct((16, n), F32))(c16, w, b)


def _ada_bwd(c16, dmod16, w):
    n = w.shape[1]
    tn = 512

    def body(c_ref, d_ref, w_ref, gw_ref, gc_ref):
        j = pl.program_id(0)

        @pl.when(j == 0)
        def _():
            gc_ref[...] = jnp.zeros_like(gc_ref)

        cc = c_ref[...]
        dm = d_ref[...]
        gw_ref[...] = _dot(cc * _sig(cc), dm, TN, prec=HI)
        gc_ref[...] += _dot(dm, w_ref[...], NT, prec=HI)

    return _pcall(body, name="ada_bwd", grid=(n // tn,),
                  in_specs=[_full((16, D)), pl.BlockSpec((16, tn), lambda j: (0, j)),
                            pl.BlockSpec((D, tn), lambda j: (0, j))],
                  out_specs=[pl.BlockSpec((D, tn), lambda j: (0, j)), _full((16, D))],
                  out_shape=[jax.ShapeDtypeStruct((D, n), F32),
                             jax.ShapeDtypeStruct((16, D), F32)])(c16, dmod16, w)


def _adam_math(w, g, m, v):
    c1 = 1.0 - ADAM_B1 ** ADAM_STEP
    c2 = 1.0 - ADAM_B2 ** ADAM_STEP
    nm = ADAM_B1 * m + (1.0 - ADAM_B1) * g
    nv = ADAM_B2 * v + (1.0 - ADAM_B2) * (g * g)
    return -ADAM_LR * ((nm / c1) / (jnp.sqrt(nv / c2) + ADAM_EPS) + ADAM_WD * w), nm, nv


def _adamw_small(ws, gs, ms, vs):
    n = len(ws)

    def body(*refs):
        for u in range(n):
            d_, nm, nv = _adam_math(refs[u][...], refs[n + u][...], refs[2 * n + u][...], refs[3 * n + u][...])
            refs[4 * n + u][...] = d_
            refs[5 * n + u][...] = nm
            refs[6 * n + u][...] = nv

    specs = [_full(w.shape) for w in ws]
    shapes = [jax.ShapeDtypeStruct(w.shape, F32) for w in ws]
    out = _pcall(body, name="adamw_small", grid=(1,), in_specs=specs * 4, out_specs=specs * 3,
                 out_shape=shapes * 3)(*ws, *gs, *ms, *vs)
    return out[:n], out[n:2 * n], out[2 * n:]


def _cctx_grad(parts, c_ctx):
    def body(p_ref, c_ref, o_ref):
        acc = p_ref[0:1, :]
        for k in range(1, 4):
            acc = acc + p_ref[k:k + 1, :]
        cc = c_ref[...]
        s = _sig(cc)
        o_ref[...] = acc * (s * (1.0 + cc * (1.0 - s)))

    return _pcall(body, name="cctx_grad", grid=(1,), in_specs=[_full(parts.shape), _full((1, D))],
                  out_specs=_full((1, D)), out_shape=jax.ShapeDtypeStruct((1, D), F32))(parts, c_ctx)


def _adamw(w, g, m, v, *, name):
    rows, cols = w.shape
    tr = next((t for t in (256, 128, 64) if rows % t == 0), rows)

    def body(w_ref, g_ref, m_ref, v_ref, d_ref, nm_ref, nv_ref):
        d_ref[...], nm_ref[...], nv_ref[...] = _adam_math(w_ref[...], g_ref[...], m_ref[...], v_ref[...])

    spec = pl.BlockSpec((tr, cols), lambda i: (i, 0))
    out = jax.ShapeDtypeStruct((rows, cols), F32)
    return _pcall(body, name=name, grid=(rows // tr,), in_specs=[spec] * 4, out_specs=[spec] * 3,
                  out_shape=[out] * 3)(w, g, m, v)


def kernel(x, c, ctx, c_ctx, w_ada, b_ada, norm_mix_w, norm_ffn_w, w_in, hgrn_lb_logits, hgrn_norm_w, q_norm_w, k_norm_w, attn_sinks, w_branch_hgrn, w_branch_attn, w_out, w_ffn_gate, w_ffn_up, w_ffn_down, loss_target, m_c_ctx, m_w_ada, m_b_ada, m_norm_mix_w, m_norm_ffn_w, m_w_in, m_hgrn_lb_logits, m_hgrn_norm_w, m_q_norm_w, m_k_norm_w, m_attn_sinks, m_w_branch_hgrn, m_w_branch_attn, m_w_out, m_w_ffn_gate, m_w_ffn_up, m_w_ffn_down, v_c_ctx, v_w_ada, v_b_ada, v_norm_mix_w, v_norm_ffn_w, v_w_in, v_hgrn_lb_logits, v_hgrn_norm_w, v_q_norm_w, v_k_norm_w, v_attn_sinks, v_w_branch_hgrn, v_w_branch_attn, v_w_out, v_w_ffn_gate, v_w_ffn_up, v_w_ffn_down):
    xi, yi, ci = _place()
    chip = 2 * xi + yi
    dev = 2 * chip + ci
    s_len = x.shape[1]

    lbrow = jnp.pad(hgrn_lb_logits.reshape(1, 512), ((0, 0), (0, D - 512)))
    blk = jnp.concatenate([c, lbrow, jnp.zeros((6, D), F32)], axis=0)
    g0, = _allgather([blk], name="ag_cond", in_vmem=True)
    c16 = jnp.concatenate([g0[:, 0], c_ctx[None], jnp.zeros((7, D), F32)], axis=0)
    lg = g0[0::2, 1, :512].reshape(4, 2, 2, 128).transpose(1, 2, 0, 3).reshape(2, 2, HGW)

    nada = w_ada.shape[2]
    b_sh = lax.dynamic_slice(b_ada, (0, chip * nada), (1, nada))
    mod_sh = _ada_fwd(c16, w_ada[0], b_sh)
    g1, = _allgather([mod_sh], name="ag_mod", in_vmem=True)
    modall = g1[0::2].transpose(1, 0, 2).reshape(16, 4 * nada)
    mod = lax.dynamic_slice(modall, (dev, 0), (1, 6 * D)).reshape(6, D)
    modc = modall[8].reshape(6, D)[:2]

    shards = [w_in[0], w_branch_hgrn[0], w_branch_attn[0], w_out[0], w_ffn_gate[0], w_ffn_up[0], w_ffn_down[0]]

    def my_half(w):
        h = w.shape[0] // 2
        return lax.dynamic_slice_in_dim(w, ci * h, h, axis=0).astype(BF16)

    gathered = _allgather([my_half(w) for w in shards], name="ag_weights", in_vmem=False)
    in4, bh4, ba4, o4, g4, u4, dn4 = [g.reshape(4, 2 * g.shape[1], g.shape[2]) for g in gathered]
    w_in_k = _to_kernel_cols(in4.transpose(1, 0, 2).reshape(D, NCOL))

    sq, gx, big, small = _local_step(
        x[0], ctx[0], loss_target[0], mod, modc, norm_mix_w, norm_ffn_w, lg, hgrn_norm_w, q_norm_w,
        k_norm_w, attn_sinks[0], w_in_k, bh4, ba4, o4.reshape(D, D), g4, u4, dn4)
    loss = lax.psum(0.5 * jnp.sum(sq) / D, ("x", "y", "c"))

    gin4 = _from_kernel_cols(big["w_in"]).reshape(D, 4, NCOL // 4).transpose(1, 0, 2)
    units = [gin4, big["w_bh"], big["w_ba"], big["w_o"].reshape(4, D // 4, D), big["w_g"], big["w_u"], big["w_dn"]]
    units = [u.reshape(4, 2, u.shape[1] // 2, u.shape[2]) for u in units]
    recvs = _rs_pair_exchange(units)
    pairs = _rs_pair_add(units, recvs, ci)
    contribs = _rs_chip_exchange(pairs)
    reds = _rs_chip_add(contribs)
    gsh = [r.reshape(2 * r.shape[1], r.shape[2]) for r in _rs_sibling_gather(reds)]

    g2, tot = _ag_small(small["raw"])
    dmodc_tot = jnp.pad(tot[6:8].reshape(1, 2 * D), ((0, 0), (0, 4 * D)))
    g_b_ada = tot[0:6].reshape(1, 6 * D) + dmodc_tot
    dmod16 = jnp.concatenate([g2[:, 0:6].reshape(8, 6 * D), dmodc_tot, jnp.zeros((7, 6 * D), F32)], axis=0)
    g_w_ada, gc_part = _ada_bwd(c16, lax.dynamic_slice(dmod16, (0, chip * nada), (16, nada)), w_ada[0])
    g3, = _allgather([gc_part[8:16]], name="ag_cctx", in_vmem=True)
    g_c_ctx = _cctx_grad(g3[0::2, 0], c_ctx[None])[0]
    g_nw1 = tot[8:9]
    g_nw2 = tot[9:10]
    g_hw = tot[10, :HGW].reshape(4, HGD).sum(0, keepdims=True)
    g_qnw = tot[10, HGW:].reshape(8, HDIM).sum(0, keepdims=True)
    g_knw = tot[11, :128].reshape(2, HDIM).sum(0, keepdims=True)
    g_sinks = tot[16:24, 0][None]
    g_lg = lax.dynamic_slice(tot[12:16, :HGW].reshape(2, 2, HGW), (0, 0, chip * 128), (2, 2, 128))

    names = ["c_ctx", "w_ada", "b_ada", "norm_mix_w", "norm_ffn_w", "w_in", "hgrn_lb_logits", "hgrn_norm_w",
             "q_norm_w", "k_norm_w", "attn_sinks", "w_branch_hgrn", "w_branch_attn", "w_out", "w_ffn_gate",
             "w_ffn_up", "w_ffn_down"]
    ws = dict(zip(names, [c_ctx, w_ada, b_ada, norm_mix_w, norm_ffn_w, w_in, hgrn_lb_logits, hgrn_norm_w,
                          q_norm_w, k_norm_w, attn_sinks, w_branch_hgrn, w_branch_attn, w_out, w_ffn_gate,
                          w_ffn_up, w_ffn_down]))
    ms = dict(zip(names, [m_c_ctx, m_w_ada, m_b_ada, m_norm_mix_w, m_norm_ffn_w, m_w_in, m_hgrn_lb_logits,
                          m_hgrn_norm_w, m_q_norm_w, m_k_norm_w, m_attn_sinks, m_w_branch_hgrn,
                          m_w_branch_attn, m_w_out, m_w_ffn_gate, m_w_ffn_up, m_w_ffn_down]))
    vs = dict(zip(names, [v_c_ctx, v_w_ada, v_b_ada, v_norm_mix_w, v_norm_ffn_w, v_w_in, v_hgrn_lb_logits,
                          v_hgrn_norm_w, v_q_norm_w, v_k_norm_w, v_attn_sinks, v_w_branch_hgrn,
                          v_w_branch_attn, v_w_out, v_w_ffn_gate, v_w_ffn_up, v_w_ffn_down]))
    grads = dict(c_ctx=g_c_ctx, w_ada=g_w_ada[None], b_ada=g_b_ada, norm_mix_w=g_nw1, norm_ffn_w=g_nw2,
                 w_in=gsh[0][None], hgrn_lb_logits=g_lg, hgrn_norm_w=g_hw, q_norm_w=g_qnw, k_norm_w=g_knw,
                 attn_sinks=g_sinks, w_branch_hgrn=gsh[1][None], w_branch_attn=gsh[2][None], w_out=gsh[3][None],
                 w_ffn_gate=gsh[4][None], w_ffn_up=gsh[5][None], w_ffn_down=gsh[6][None])
    big_names = ["w_ada", "w_in", "w_branch_hgrn", "w_branch_attn", "w_out", "w_ffn_gate", "w_ffn_up", "w_ffn_down"]
    small_names = [n for n in names if n not in big_names]
    delta, new_m, new_v = {}, {}, {}
    for n in big_names:
        d_, m_, v_ = _adamw(ws[n][0], grads[n][0], ms[n][0], vs[n][0], name="adamw_" + n)
        delta[n], new_m[n], new_v[n] = d_[None], m_[None], v_[None]

    def two_d(a):
        return a.reshape(1, -1) if a.ndim == 1 else a

    sd, sm_, sv = _adamw_small(*[[two_d(d[n]) for n in small_names] for d in (ws, grads, ms, vs)])
    for i, n in enumerate(small_names):
        for dst, src in ((delta, sd), (new_m, sm_), (new_v, sv)):
            dst[n] = src[i].reshape(ws[n].shape)
    return (loss, gx[None], *[grads[n] for n in names], *[delta[n] for n in names],
            *[new_m[n] for n in names], *[new_v[n] for n in names])
```

```python
import functools

import numpy as np
import jax
import jax.numpy as jnp
from jax import lax
from jax.experimental import pallas as pl
from jax.experimental.pallas import tpu as pltpu

F32 = jnp.float32
BF16 = jnp.bfloat16
HI = lax.Precision.HIGHEST
MESH = pl.DeviceIdType.MESH

D = 1024
L = 256
TM = 256
HGW = 512
HGD = 128
CH = 32
ATW = 512
HDIM = 64
BLK = 128
GRID_W = 64
DFF = 2816
NCOL = 5376
EPS = 1e-6
ROPE_THETA = 10000.0

C_FB, C_INP, C_QHG, C_FF = 0, 1, 2, 3
C_GATES = 1
C_GHG, C_QRAW = 8, 9
C_KV = 20
C_QKV = 6

ADAM_LR, ADAM_B1, ADAM_B2, ADAM_EPS, ADAM_WD, ADAM_STEP = 0.001, 0.9, 0.999, 1e-08, 0.01, 10

NN = (((1,), (0,)), ((), ()))
NT = (((1,), (1,)), ((), ()))
TN = (((0,), (0,)), ((), ()))


def _dot(a, b, dims=NN, prec=None):
    return lax.dot_general(a, b, dims, precision=prec, preferred_element_type=F32)


def _bdot(a, b, dims=NN):
    return _dot(a.astype(BF16), b.astype(BF16), dims)


def _sig(x):
    return 1.0 / (1.0 + jnp.exp(-x))


def _pcall(body, *, name, grid, in_specs, out_specs, out_shape, scratch=(), aliases=None, vmem_mb=48):
    return pl.pallas_call(
        body, name=name, grid=grid, in_specs=in_specs, out_specs=out_specs, out_shape=out_shape,
        scratch_shapes=list(scratch), input_output_aliases=aliases or {},
        compiler_params=pltpu.CompilerParams(
            dimension_semantics=("arbitrary",) * len(grid), vmem_limit_bytes=vmem_mb << 20))


def _full(shape):
    nd = len(shape)
    return pl.BlockSpec(shape, lambda *_: (0,) * nd)


ANY = pl.BlockSpec(memory_space=pl.ANY)


def _mm(a, b, *, name, mode="nn", out_dtype=F32, tm, tn, tk):
    if mode == "nn":
        (m, k), (k2, n) = a.shape, b.shape
    elif mode == "nt":
        (m, k), (n, k2) = a.shape, b.shape
    else:
        (k, m), (k2, n) = a.shape, b.shape
    assert k == k2 and m % tm == 0 and n % tn == 0 and k % tk == 0, (name, a.shape, b.shape)
    nk = k // tk
    dims = {"nn": NN, "nt": NT, "tn": TN}[mode]

    def body(a_ref, b_ref, o_ref, acc):
        kk = pl.program_id(2)

        @pl.when(kk == 0)
        def _():
            acc[...] = jnp.zeros_like(acc)

        acc[...] += _bdot(a_ref[...], b_ref[...], dims)

        @pl.when(kk == nk - 1)
        def _():
            o_ref[...] = acc[...].astype(out_dtype)

    a_spec = (pl.BlockSpec((tk, tm), lambda i, j, kk: (kk, i)) if mode == "tn"
              else pl.BlockSpec((tm, tk), lambda i, j, kk: (i, kk)))
    b_spec = (pl.BlockSpec((tn, tk), lambda i, j, kk: (j, kk)) if mode == "nt"
              else pl.BlockSpec((tk, tn), lambda i, j, kk: (kk, j)))
    return _pcall(body, name=name, grid=(m // tm, n // tn, nk), in_specs=[a_spec, b_spec],
                  out_specs=pl.BlockSpec((tm, tn), lambda i, j, kk: (i, j)),
                  out_shape=jax.ShapeDtypeStruct((m, n), out_dtype),
                  scratch=[pltpu.VMEM((tm, tn), F32)])(a, b)


def _modulate(xin, nw, ss, *, name, sel):
    rows = xin.shape[0]

    def body(x_ref, nw_ref, ss_ref, h_ref):
        x = x_ref[...]
        r = lax.rsqrt(jnp.mean(x * x, axis=-1, keepdims=True) + EPS)
        s = ss_ref[0]
        h_ref[...] = ((x * r * nw_ref[...]) * (1.0 + s[1:2]) + s[0:1]).astype(BF16)

    return _pcall(body, name=name, grid=(rows // TM,),
                  in_specs=[pl.BlockSpec((TM, D), lambda i: (i, 0)), _full((1, D)),
                            pl.BlockSpec((1, 2, D), lambda i: (sel(i), 0, 0))],
                  out_specs=pl.BlockSpec((TM, D), lambda i: (i, 0)),
                  out_shape=jax.ShapeDtypeStruct((rows, D), BF16))(xin, nw, ss)


def _norm_bwd_rows(x, dh, nw, scale):
    r = lax.rsqrt(jnp.mean(x * x, axis=-1, keepdims=True) + EPS)
    xh = x * r
    dxh = dh * ((1.0 + scale) * nw)
    dx = r * (dxh - xh * jnp.mean(dxh * xh, axis=-1, keepdims=True))
    return dx, xh


def _res1_mod2(x, ao, g1, nw2, ss2):
    s_len = x.shape[0]

    def body(x_ref, ao_ref, g_ref, nw_ref, ss_ref, x1_ref, h_ref):
        x1 = x_ref[...] + g_ref[...] * ao_ref[...]
        x1_ref[...] = x1
        r = lax.rsqrt(jnp.mean(x1 * x1, axis=-1, keepdims=True) + EPS)
        s = ss_ref[0]
        h_ref[...] = ((x1 * r * nw_ref[...]) * (1.0 + s[1:2]) + s[0:1]).astype(BF16)

    row = pl.BlockSpec((TM, D), lambda i: (i, 0))
    return _pcall(body, name="res1_mod2", grid=(s_len // TM,),
                  in_specs=[row, row, _full((1, D)), _full((1, D)), _full((1, 2, D))],
                  out_specs=[row, row],
                  out_shape=[jax.ShapeDtypeStruct((s_len, D), F32),
                             jax.ShapeDtypeStruct((s_len, D), BF16)])(x, ao, g1, nw2, ss2)


TS = 512


def _acc_call(body, *, name, grid, in_specs, out_specs, out_shape, acc_shapes, args):
    return _pcall(body, name=name, grid=grid, in_specs=in_specs, out_specs=out_specs, out_shape=out_shape,
                  scratch=[pltpu.VMEM(s, F32) for s in acc_shapes])(*args)


def _mm_cs(a, w4, *, name):
    m, k = a.shape
    _, _, ns = w4.shape

    def body(a_ref, w_ref, o_ref):
        o_ref[...] = _bdot(a_ref[...], w_ref[0])

    return _pcall(body, name=name, grid=(m // TS, 4),
                  in_specs=[pl.BlockSpec((TS, k), lambda i, j: (i, 0)),
                            pl.BlockSpec((1, k, ns), lambda i, j: (j, 0, 0))],
                  out_specs=pl.BlockSpec((TS, ns), lambda i, j: (i, j)),
                  out_shape=jax.ShapeDtypeStruct((m, 4 * ns), F32))(a, w4)


def _mm_cs_nt(a, w4, *, name):
    m = a.shape[0]
    _, k, ns = w4.shape

    def body(a_ref, w_ref, o_ref, acc):
        j = pl.program_id(1)

        @pl.when(j == 0)
        def _():
            acc[...] = jnp.zeros_like(acc)

        acc[...] += _bdot(a_ref[...], w_ref[0], NT)

        @pl.when(j == 3)
        def _():
            o_ref[...] = acc[...]

    return _acc_call(body, name=name, grid=(m // TS, 4),
                     in_specs=[pl.BlockSpec((TS, ns), lambda i, j: (i, j)),
                               pl.BlockSpec((1, k, ns), lambda i, j: (j, 0, 0))],
                     out_specs=pl.BlockSpec((TS, k), lambda i, j: (i, 0)),
                     out_shape=jax.ShapeDtypeStruct((m, k), F32), acc_shapes=[(TS, k)], args=(a, w4))


def _mm_cs_tn(a, b, ns, *, name):
    s_len, k = a.shape
    nk = s_len // TS

    def body(a_ref, b_ref, o_ref, acc):
        t = pl.program_id(1)

        @pl.when(t == 0)
        def _():
            acc[...] = jnp.zeros_like(acc)

        acc[...] += _bdot(a_ref[...], b_ref[...], TN)

        @pl.when(t == nk - 1)
        def _():
            o_ref[0] = acc[...]

    return _acc_call(body, name=name, grid=(4, nk),
                     in_specs=[pl.BlockSpec((TS, k), lambda j, t: (t, 0)),
                               pl.BlockSpec((TS, ns), lambda j, t: (t, j))],
                     out_specs=pl.BlockSpec((1, k, ns), lambda j, t: (j, 0, 0)),
                     out_shape=jax.ShapeDtypeStruct((4, k, ns), F32), acc_shapes=[(k, ns)], args=(a, b))


def _ffn_up(h2, g4, u4):
    s_len = h2.shape[0]
    ns = g4.shape[2]

    def body(h_ref, g_ref, u_ref, a_ref, b_ref, z_ref):
        h = h_ref[...]
        a = _bdot(h, g_ref[0])
        b = _bdot(h, u_ref[0])
        a_ref[0] = a
        b_ref[0] = b
        z_ref[0] = (a * _sig(a) * b).astype(BF16)

    w = pl.BlockSpec((1, D, ns), lambda i, j: (j, 0, 0))
    o = pl.BlockSpec((1, TS, ns), lambda i, j: (j, i, 0))
    f = jax.ShapeDtypeStruct((4, s_len, ns), F32)
    return _pcall(body, name="ffn_up", grid=(s_len // TS, 4),
                  in_specs=[pl.BlockSpec((TS, D), lambda i, j: (i, 0)), w, w], out_specs=[o, o, o],
                  out_shape=[f, f, jax.ShapeDtypeStruct((4, s_len, ns), BF16)])(h2, g4, u4)


def _ffn_down(z4, dn4):
    _, s_len, ns = z4.shape

    def body(z_ref, w_ref, o_ref, acc):
        j = pl.program_id(1)

        @pl.when(j == 0)
        def _():
            acc[...] = jnp.zeros_like(acc)

        acc[...] += _bdot(z_ref[0], w_ref[0])

        @pl.when(j == 3)
        def _():
            o_ref[...] = acc[...]

    return _acc_call(body, name="ffn_down", grid=(s_len // TS, 4),
                     in_specs=[pl.BlockSpec((1, TS, ns), lambda i, j: (j, i, 0)),
                               pl.BlockSpec((1, ns, D), lambda i, j: (j, 0, 0))],
                     out_specs=pl.BlockSpec((TS, D), lambda i, j: (i, 0)),
                     out_shape=jax.ShapeDtypeStruct((s_len, D), F32), acc_shapes=[(TS, D)], args=(z4, dn4))


def _ffn_dz(dyb, dn4, a4, b4):
    _, s_len, ns = a4.shape

    def body(dy_ref, w_ref, a_ref, b_ref, da_ref, db_ref):
        dz = _bdot(dy_ref[...], w_ref[0], NT)
        a = a_ref[0]
        s = _sig(a)
        da_ref[0] = (dz * b_ref[0] * (s * (1.0 + a * (1.0 - s)))).astype(BF16)
        db_ref[0] = (dz * (a * s)).astype(BF16)

    t = pl.BlockSpec((1, TS, ns), lambda i, j: (j, i, 0))
    o = jax.ShapeDtypeStruct((4, s_len, ns), BF16)
    return _pcall(body, name="ffn_dz", grid=(s_len // TS, 4),
                  in_specs=[pl.BlockSpec((TS, D), lambda i, j: (i, 0)),
                            pl.BlockSpec((1, ns, D), lambda i, j: (j, 0, 0)), t, t],
                  out_specs=[t, t], out_shape=[o, o])(dyb, dn4, a4, b4)


def _ffn_gdn(z4, dyb):
    _, s_len, ns = z4.shape
    nk = s_len // TS

    def body(z_ref, dy_ref, o_ref, acc):
        t = pl.program_id(1)

        @pl.when(t == 0)
        def _():
            acc[...] = jnp.zeros_like(acc)

        acc[...] += _bdot(z_ref[0], dy_ref[...], TN)

        @pl.when(t == nk - 1)
        def _():
            o_ref[0] = acc[...]

    return _acc_call(body, name="ffn_gdn", grid=(4, nk),
                     in_specs=[pl.BlockSpec((1, TS, ns), lambda j, t: (j, t, 0)),
                               pl.BlockSpec((TS, D), lambda j, t: (t, 0))],
                     out_specs=pl.BlockSpec((1, ns, D), lambda j, t: (j, 0, 0)),
                     out_shape=jax.ShapeDtypeStruct((4, ns, D), F32), acc_shapes=[(ns, D)], args=(z4, dyb))


def _ffn_dh2(da4, db4, g4, u4):
    _, s_len, ns = da4.shape

    def body(da_ref, db_ref, g_ref, u_ref, o_ref, acc):
        j = pl.program_id(1)

        @pl.when(j == 0)
        def _():
            acc[...] = jnp.zeros_like(acc)

        acc[...] += _bdot(da_ref[0], g_ref[0], NT) + _bdot(db_ref[0], u_ref[0], NT)

        @pl.when(j == 3)
        def _():
            o_ref[...] = acc[...]

    t = pl.BlockSpec((1, TS, ns), lambda i, j: (j, i, 0))
    w = pl.BlockSpec((1, D, ns), lambda i, j: (j, 0, 0))
    return _acc_call(body, name="ffn_dh2", grid=(s_len // TS, 4), in_specs=[t, t, w, w],
                     out_specs=pl.BlockSpec((TS, D), lambda i, j: (i, 0)),
                     out_shape=jax.ShapeDtypeStruct((s_len, D), F32), acc_shapes=[(TS, D)],
                     args=(da4, db4, g4, u4))


def _ffn_ggu(h2, da4, db4):
    _, s_len, ns = da4.shape
    nk = s_len // TS

    def body(h_ref, da_ref, db_ref, gg_ref, gu_ref, acc_g, acc_u):
        t = pl.program_id(1)

        @pl.when(t == 0)
        def _():
            acc_g[...] = jnp.zeros_like(acc_g)
            acc_u[...] = jnp.zeros_like(acc_u)

        h = h_ref[...]
        acc_g[...] += _bdot(h, da_ref[0], TN)
        acc_u[...] += _bdot(h, db_ref[0], TN)

        @pl.when(t == nk - 1)
        def _():
            gg_ref[0] = acc_g[...]
            gu_ref[0] = acc_u[...]

    d = pl.BlockSpec((1, TS, ns), lambda j, t: (j, t, 0))
    o = pl.BlockSpec((1, D, ns), lambda j, t: (j, 0, 0))
    f = jax.ShapeDtypeStruct((4, D, ns), F32)
    return _acc_call(body, name="ffn_ggu", grid=(4, nk),
                     in_specs=[pl.BlockSpec((TS, D), lambda j, t: (t, 0)), d, d], out_specs=[o, o],
                     out_shape=[f, f], acc_shapes=[(D, ns), (D, ns)], args=(h2, da4, db4))


def _loss_head(x1, y, g2, tgt):
    s_len = x1.shape[0]

    def body(x1_ref, y_ref, g_ref, t_ref, sq_ref, dx2_ref, dyb_ref, dg_ref):
        i = pl.program_id(0)

        @pl.when(i == 0)
        def _():
            sq_ref[...] = jnp.zeros_like(sq_ref)
            dg_ref[...] = jnp.zeros_like(dg_ref)

        y_ = y_ref[...]
        g = g_ref[...]
        e = x1_ref[...] + g * y_ - t_ref[...]
        sq_ref[...] += jnp.sum(e * e, axis=0, keepdims=True)
        dx2 = e * (1.0 / D)
        dx2_ref[...] = dx2
        dyb_ref[...] = (g * dx2).astype(BF16)
        dg_ref[...] += jnp.sum(dx2 * y_, axis=0, keepdims=True)

    row = pl.BlockSpec((TM, D), lambda i: (i, 0))
    vec = _full((1, D))
    return _pcall(body, name="loss_head", grid=(s_len // TM,),
                  in_specs=[row, row, vec, row], out_specs=[vec, row, row, vec],
                  out_shape=[jax.ShapeDtypeStruct((1, D), F32), jax.ShapeDtypeStruct((s_len, D), F32),
                             jax.ShapeDtypeStruct((s_len, D), BF16),
                             jax.ShapeDtypeStruct((1, D), F32)])(x1, y, g2, tgt)


def _mod2_bwd(x1, dh2, dx2, ao, nw2, ss2, g1):
    s_len = x1.shape[0]

    def body(x1_ref, dh_ref, dx2_ref, ao_ref, nw_ref, ss_ref, g_ref,
             dx1_ref, da_ref, dss_ref, dnw_ref, dg_ref):
        i = pl.program_id(0)

        @pl.when(i == 0)
        def _():
            dss_ref[...] = jnp.zeros_like(dss_ref)
            dnw_ref[...] = jnp.zeros_like(dnw_ref)
            dg_ref[...] = jnp.zeros_like(dg_ref)

        dh = dh_ref[...]
        nw = nw_ref[...]
        scale = ss_ref[0][1:2]
        dxn, xh = _norm_bwd_rows(x1_ref[...], dh, nw, scale)
        dx1 = dx2_ref[...] + dxn
        dx1_ref[...] = dx1
        da_ref[...] = (g_ref[...] * dx1).astype(BF16)
        dg_ref[...] += jnp.sum(dx1 * ao_ref[...], axis=0, keepdims=True)
        dsh = jnp.sum(dh, axis=0, keepdims=True)
        dsc = jnp.sum(dh * xh * nw, axis=0, keepdims=True)
        dss_ref[...] += jnp.concatenate([dsh, dsc], axis=0)
        dnw_ref[...] += jnp.sum(dh * xh * (1.0 + scale), axis=0, keepdims=True)

    row = pl.BlockSpec((TM, D), lambda i: (i, 0))
    vec = _full((1, D))
    return _pcall(body, name="mod2_bwd", grid=(s_len // TM,),
                  in_specs=[row, row, row, row, vec, _full((1, 2, D)), vec],
                  out_specs=[row, row, _full((2, D)), vec, vec],
                  out_shape=[jax.ShapeDtypeStruct((s_len, D), F32), jax.ShapeDtypeStruct((s_len, D), BF16),
                             jax.ShapeDtypeStruct((2, D), F32), jax.ShapeDtypeStruct((1, D), F32),
                             jax.ShapeDtypeStruct((1, D), F32)])(x1, dh2, dx2, ao, nw2, ss2, g1)


def _mod1_bwd(tok, dh, dx1, nw1, ss1):
    tt = tok.shape[0]
    s_len = dx1.shape[0]

    def body(t_ref, dh_ref, dx1_ref, nw_ref, ss_ref, dx_ref, dss_ref, dnw_ref):
        i = pl.program_id(0)

        @pl.when(i == 0)
        def _():
            dnw_ref[...] = jnp.zeros_like(dnw_ref)

        @pl.when(i <= 1)
        def _():
            dss_ref[...] = jnp.zeros_like(dss_ref)

        dh_ = dh_ref[...]
        nw = nw_ref[...]
        scale = ss_ref[0][1:2]
        dxn, xh = _norm_bwd_rows(t_ref[...], dh_, nw, scale)

        @pl.when(i >= 1)
        def _():
            dx_ref[...] = dx1_ref[...] + dxn

        dsh = jnp.sum(dh_, axis=0, keepdims=True)
        dsc = jnp.sum(dh_ * xh * nw, axis=0, keepdims=True)
        dss_ref[...] += jnp.concatenate([dsh, dsc], axis=0)[None]
        dnw_ref[...] += jnp.sum(dh_ * xh * (1.0 + scale), axis=0, keepdims=True)

    row = pl.BlockSpec((TM, D), lambda i: (i, 0))
    lat = pl.BlockSpec((TM, D), lambda i: (jnp.maximum(i - 1, 0), 0))
    sel = pl.BlockSpec((1, 2, D), lambda i: (jnp.minimum(i, 1), 0, 0))
    return _pcall(body, name="mod1_bwd", grid=(tt // TM,),
                  in_specs=[row, row, lat, _full((1, D)), sel],
                  out_specs=[lat, sel, _full((1, D))],
                  out_shape=[jax.ShapeDtypeStruct((s_len, D), F32), jax.ShapeDtypeStruct((2, 2, D), F32),
                             jax.ShapeDtypeStruct((1, D), F32)])(tok, dh, dx1, nw1, ss1)


def _tri(rev, transpose=False):
    r = lax.broadcasted_iota(jnp.int32, (CH, CH), 0)
    c = lax.broadcasted_iota(jnp.int32, (CH, CH), 1)
    lower = (c >= r) if (rev != transpose) else (c <= r)
    return lower


def _hgrn_gate(fl, qraw, lg):
    lb = 1.0 / (1.0 + jnp.exp(lg[1:2] - lg[0:1]))
    sg = _sig(fl)
    f = lb + (1.0 - lb) * sg
    q = qraw * _sig(qraw) * (HGD ** -0.5)
    return lb, sg, f, q


def _hgrn_fwd(p, lg, *, rev):
    tt = p.shape[0]
    nt = tt // TM
    ncht = TM // CH
    d = 1 if rev else 0

    def tile_of(s):
        return jnp.where(s == 0, 0, nt - s) if rev else s

    def body(f_ref, inp_ref, q_ref, lg_ref, o_ref, st_ref, lf_s, k_s, q_s, state):
        s = pl.program_id(0)

        @pl.when(s == 0)
        def _():
            state[...] = jnp.zeros_like(state)

        _, _, f, q = _hgrn_gate(f_ref[...], q_ref[...], lg_ref[0])
        lf_s[...] = jnp.log(f)
        k_s[...] = 1.0 - f
        q_s[...] = q
        tri = _tri(rev)
        trif = tri.astype(F32)

        def chunk(cc, carry):
            c = (ncht - 1 - cc) if rev else cc
            r0 = pl.multiple_of(c * CH, CH)
            lf = lf_s[pl.ds(r0, CH), :]
            cum = _dot(trif, lf, prec=HI)
            tot = jnp.sum(lf, axis=0, keepdims=True)
            qd = q_s[pl.ds(r0, CH), :] * jnp.exp(cum)
            kk = k_s[pl.ds(r0, CH), :]
            kd = kk * jnp.exp(-cum)
            ke = kk * jnp.exp(tot - cum)
            et = jnp.exp(tot)
            v = inp_ref[pl.ds(r0, CH), :]
            outs = []
            for h in range(4):
                sl = slice(h * HGD, (h + 1) * HGD)
                st0 = state[h]
                st_ref[c, h] = st0
                pm = jnp.where(tri, _bdot(qd[:, sl], kd[:, sl], NT), 0.0)
                outs.append(_bdot(pm, v[:, sl]) + _bdot(qd[:, sl], st0, NT))
                state[h] = st0 * et[:, sl] + _bdot(v[:, sl], ke[:, sl], TN)
            o_ref[pl.ds(r0, CH), :] = jnp.concatenate(outs, axis=1)
            return carry

        lax.fori_loop(0, ncht, chunk, 0)

    def col(cb):
        return pl.BlockSpec((TM, HGW), lambda s: (tile_of(s), cb))

    return _pcall(
        body, name="hgrn_fwd_rev" if rev else "hgrn_fwd", grid=(nt,),
        in_specs=[col(C_FB if rev else C_FF), col(C_INP), col(C_QHG),
                  pl.BlockSpec((1, 2, HGW), lambda s: (d, 0, 0))],
        out_specs=[pl.BlockSpec((TM, HGW), lambda s: (tile_of(s), 0)),
                   pl.BlockSpec((ncht, 4, HGD, HGD), lambda s: (tile_of(s), 0, 0, 0))],
        out_shape=[jax.ShapeDtypeStruct((tt, HGW), F32),
                   jax.ShapeDtypeStruct((nt * ncht, 4, HGD, HGD), F32)],
        scratch=[pltpu.VMEM((TM, HGW), F32)] * 3 + [pltpu.VMEM((4, HGD, HGD), F32)])(p, p, p, lg)


def _hgrn_bwd(p, lg, do, st, dp, prev, *, rev):
    tt = p.shape[0]
    nt = tt // TM
    ncht = TM // CH
    d = 1 if rev else 0
    second = prev is not None

    def tile_of(s):
        return jnp.where(s == nt - 1, 0, s + 1) if rev else nt - 1 - s

    def body(*refs):
        if second:
            (f_ref, inp_ref, q_ref, lg_ref, do_ref, st_ref, dvp_ref, dqp_ref, _dp_in,
             dp_ref, dlg_ref, lf_s, k_s, q_s, do_s, dq_s, dk_s, dv_s, dlf_s, dstate) = refs
        else:
            (f_ref, inp_ref, q_ref, lg_ref, do_ref, st_ref, _dp_in,
             dp_ref, dv_ref, dq_ref, dlg_ref, lf_s, k_s, q_s, do_s, dq_s, dk_s, dv_s, dlf_s, dstate) = refs
        s = pl.program_id(0)
        tile = tile_of(s)

        @pl.when(s == 0)
        def _():
            dstate[...] = jnp.zeros_like(dstate)
            dlg_ref[...] = jnp.zeros_like(dlg_ref)

        qraw = q_ref[...]
        lb, sg, f, q = _hgrn_gate(f_ref[...], qraw, lg_ref[0])
        lf_s[...] = jnp.log(f)
        k_s[...] = 1.0 - f
        q_s[...] = q
        do_s[...] = jnp.where(tile == 0, 0.0, do_ref[...])
        tri = _tri(rev)
        trit = _tri(rev, transpose=True)
        tritf = trit.astype(F32)
        trif = tri.astype(F32)

        def chunk(cc, carry):
            c = cc if rev else (ncht - 1 - cc)
            r0 = pl.multiple_of(c * CH, CH)
            lf = lf_s[pl.ds(r0, CH), :]
            cum = _dot(trif, lf, prec=HI)
            tot = jnp.sum(lf, axis=0, keepdims=True)
            ea = jnp.exp(cum)
            eb = jnp.exp(-cum)
            ee = jnp.exp(tot - cum)
            et = jnp.exp(tot)
            qd = q_s[pl.ds(r0, CH), :] * ea
            kk = k_s[pl.ds(r0, CH), :]
            kd = kk * eb
            ke = kk * ee
            v = inp_ref[pl.ds(r0, CH), :]
            doc = do_s[pl.ds(r0, CH), :]
            dq_l, dk_l, dv_l, dcum_l, dtot_l = [], [], [], [], []
            for h in range(4):
                sl = slice(h * HGD, (h + 1) * HGD)
                qd_, kd_, ke_, v_, do_ = qd[:, sl], kd[:, sl], ke[:, sl], v[:, sl], doc[:, sl]
                st0 = st_ref[c, h]
                ds1 = dstate[h]
                pmt = jnp.where(trit, _bdot(kd_, qd_, NT), 0.0)
                dpm = jnp.where(tri, _bdot(do_, v_, NT), 0.0)
                dpmt = jnp.where(trit, _bdot(v_, do_, NT), 0.0)
                dv = _bdot(pmt, do_) + _bdot(ke_, ds1, NT)
                dqd = _bdot(dpm, kd_) + _bdot(do_, st0)
                dkd = _bdot(dpmt, qd_)
                dke = _bdot(v_, ds1)
                dstate[h] = ds1 * et[:, sl] + _bdot(do_, qd_, TN)
                dtot_l.append(jnp.sum(ds1 * st0, axis=0, keepdims=True) * et[:, sl]
                              + jnp.sum(dke * ke_, axis=0, keepdims=True))
                dq_l.append(dqd * ea[:, sl])
                dk_l.append(dkd * eb[:, sl] + dke * ee[:, sl])
                dv_l.append(dv)
                dcum_l.append(dqd * qd_ - dkd * kd_ - dke * ke_)
            dcum = jnp.concatenate(dcum_l, axis=1)
            dlf = _dot(tritf, dcum, prec=HI) + jnp.concatenate(dtot_l, axis=1)
            dq_s[pl.ds(r0, CH), :] = jnp.concatenate(dq_l, axis=1)
            dk_s[pl.ds(r0, CH), :] = jnp.concatenate(dk_l, axis=1)
            dv_s[pl.ds(r0, CH), :] = jnp.concatenate(dv_l, axis=1)
            dlf_s[pl.ds(r0, CH), :] = dlf
            return carry

        lax.fori_loop(0, ncht, chunk, 0)

        df = dlf_s[...] / f - dk_s[...]
        dfl = df * (1.0 - lb) * sg * (1.0 - sg)
        dlb = jnp.sum(df * (1.0 - sg), axis=0, keepdims=True)
        dl0 = dlb * lb * (1.0 - lb)
        dlg_ref[...] += jnp.concatenate([dl0, -dl0], axis=0)[None]
        if second:
            sq = _sig(qraw)
            dqr = (dqp_ref[...] + dq_s[...]) * (HGD ** -0.5) * (sq * (1.0 + qraw * (1.0 - sq)))
            dp_ref[...] = jnp.concatenate([dfl, dvp_ref[...] + dv_s[...], dqr], axis=1).astype(BF16)
        else:
            dp_ref[...] = dfl.astype(BF16)
            dv_ref[...] = dv_s[...]
            dq_ref[...] = dq_s[...]

    def col(cb):
        return pl.BlockSpec((TM, HGW), lambda s: (tile_of(s), cb))

    tok = pl.BlockSpec((TM, HGW), lambda s: (tile_of(s), 0))
    in_specs = [col(C_FB if rev else C_FF), col(C_INP), col(C_QHG),
                pl.BlockSpec((1, 2, HGW), lambda s: (d, 0, 0)),
                pl.BlockSpec((TM, HGW), lambda s: (jnp.maximum(tile_of(s) - 1, 0), 0)),
                pl.BlockSpec((ncht, 4, HGD, HGD), lambda s: (tile_of(s), 0, 0, 0))]
    args = [p, p, p, lg, do, st]
    dlg_spec = _full((1, 2, HGW))
    dlg_shape = jax.ShapeDtypeStruct((1, 2, HGW), F32)
    if second:
        in_specs += [tok, tok]
        args += [prev[0], prev[1]]
        out_specs = [pl.BlockSpec((TM, 3 * HGW), lambda s: (tile_of(s), 0)), dlg_spec]
        out_shape = [jax.ShapeDtypeStruct(dp.shape, BF16), dlg_shape]
    else:
        out_specs = [pl.BlockSpec((TM, HGW), lambda s: (tile_of(s), C_FB if rev else C_FF)), tok, tok, dlg_spec]
        out_shape = [jax.ShapeDtypeStruct(dp.shape, BF16), jax.ShapeDtypeStruct((tt, HGW), F32),
                     jax.ShapeDtypeStruct((tt, HGW), F32), dlg_shape]
    in_specs.append(ANY)
    args.append(dp)
    return _pcall(body, name="hgrn_bwd_rev" if rev else "hgrn_bwd", grid=(nt,),
                  in_specs=in_specs, out_specs=out_specs, out_shape=out_shape,
                  scratch=[pltpu.VMEM((TM, HGW), F32)] * 8 + [pltpu.VMEM((4, HGD, HGD), F32)],
                  aliases={len(args) - 1: 0})(*args)


def _head_rms(o, w, nheads):
    outs = []
    for h in range(nheads):
        oh = o[:, h * HGD:(h + 1) * HGD]
        outs.append(oh * lax.rsqrt(jnp.mean(oh * oh, axis=-1, keepdims=True) + EPS))
    return jnp.concatenate(outs, axis=1)


def _readout(o0, o1, p, hw4):
    s_len = o0.shape[0] - L

    def body(o0_ref, o1_ref, g_ref, w_ref, y_ref):
        xh = _head_rms(o0_ref[...] + o1_ref[...], None, 4)
        g = g_ref[...]
        y_ref[...] = (xh * w_ref[...] * (g * _sig(g))).astype(BF16)

    lat = pl.BlockSpec((TM, HGW), lambda i: (i + 1, 0))
    return _pcall(body, name="readout", grid=(s_len // TM,),
                  in_specs=[lat, lat, pl.BlockSpec((TM, HGW), lambda i: (i + 1, C_GHG)), _full((1, HGW))],
                  out_specs=pl.BlockSpec((TM, HGW), lambda i: (i, 0)),
                  out_shape=jax.ShapeDtypeStruct((s_len, HGW), BF16))(o0, o1, p, hw4)


def _readout_bwd(o0, o1, p, hw4, dy, dp):
    tt = o0.shape[0]
    s_len = tt - L

    def body(o0_ref, o1_ref, g_ref, w_ref, dy_ref, _dp_in, dp_ref, do_ref, dw_ref):
        i = pl.program_id(0)

        @pl.when(i == 0)
        def _():
            dw_ref[...] = jnp.zeros_like(dw_ref)
            dp_ref[...] = jnp.zeros_like(dp_ref)

        @pl.when(i >= 1)
        def _():
            o = o0_ref[...] + o1_ref[...]
            g = g_ref[...]
            w = w_ref[...]
            sg = _sig(g)
            dy_ = dy_ref[...]
            dsw = dy_ * (g * sg)
            outs, xhs = [], []
            for h in range(4):
                sl = slice(h * HGD, (h + 1) * HGD)
                oh = o[:, sl]
                r = lax.rsqrt(jnp.mean(oh * oh, axis=-1, keepdims=True) + EPS)
                xh = oh * r
                dxh = dsw[:, sl] * w[:, sl]
                outs.append(r * (dxh - xh * jnp.mean(dxh * xh, axis=-1, keepdims=True)))
                xhs.append(xh)
            xh = jnp.concatenate(xhs, axis=1)
            do_ref[...] = jnp.concatenate(outs, axis=1)
            dp_ref[...] = (dy_ * xh * w * (sg * (1.0 + g * (1.0 - sg)))).astype(BF16)
            dw_ref[...] += jnp.sum(dsw * xh, axis=0, keepdims=True)

    tok = pl.BlockSpec((TM, HGW), lambda i: (i, 0))
    lat = pl.BlockSpec((TM, HGW), lambda i: (jnp.maximum(i - 1, 0), 0))
    return _pcall(body, name="readout_bwd", grid=(tt // TM,),
                  in_specs=[tok, tok, pl.BlockSpec((TM, HGW), lambda i: (i, C_GHG)), _full((1, HGW)), lat, ANY],
                  out_specs=[pl.BlockSpec((TM, HGW), lambda i: (i, C_GHG)), lat, _full((1, HGW))],
                  out_shape=[jax.ShapeDtypeStruct(dp.shape, BF16), jax.ShapeDtypeStruct((s_len, HGW), F32),
                             jax.ShapeDtypeStruct((1, HGW), F32)],
                  aliases={5: 0})(o0, o1, p, hw4, dy, dp)


def _rope_tables(s_len):
    t = np.arange(s_len)
    inv = ROPE_THETA ** (-np.arange(0, 32, 2, dtype=np.float64) / 32)
    def half(pos):
        ang = pos[:, None].astype(np.float64) * inv[None, :]
        return (np.concatenate([np.cos(ang), np.cos(ang)], 1), np.concatenate([-np.sin(ang), np.sin(ang)], 1))
    cr, sr = half(t // GRID_W)
    cc, sc = half(t % GRID_W)
    cos = np.concatenate([cr, cc, cr, cc], 1)
    sin = np.concatenate([sr, sc, sr, sc], 1)
    cos = np.concatenate([np.ones((L, 128)), cos], 0)
    sin = np.concatenate([np.zeros((L, 128)), sin], 0)
    return jnp.asarray(cos, F32), jnp.asarray(sin, F32)


def _blockdiag(n, w):
    i = np.arange(n)
    return jnp.asarray((i[:, None] // w == i[None, :] // w) / float(w), F32)


def _dup_matrix():
    m = np.zeros((128, 512), np.float32)
    for g in range(2):
        for j in range(4):
            for dd in range(HDIM):
                m[64 * g + dd, 256 * g + 64 * j + dd] = 1.0
    return m


def _rot(x):
    n = x.shape[1]
    lane = lax.broadcasted_iota(jnp.int32, x.shape, 1)
    return jnp.where((lane % 32) < 16, pltpu.roll(x, n - 16, 1), pltpu.roll(x, 16, 1))


def _qk_prep(p, cos, sin, qnw8, knw2, bd512, bd128, dup):
    tt = p.shape[0]

    def body(q_ref, kv_ref, cos_ref, sin_ref, qw_ref, kw_ref, b5_ref, b1_ref, dup_ref,
             qr_ref, k4_ref, v4_ref):
        cos_, sin_ = cos_ref[...], sin_ref[...]
        q = q_ref[...]
        qn = q * lax.rsqrt(_dot(q * q, b5_ref[...], prec=HI) + EPS) * qw_ref[...]
        cos4 = jnp.concatenate([cos_] * 4, axis=1)
        sin4 = jnp.concatenate([sin_] * 4, axis=1)
        qr_ref[...] = ((qn * cos4 + _rot(qn) * sin4) * (HDIM ** -0.5)).astype(BF16)
        kv = kv_ref[...]
        k, v = kv[:, :128], kv[:, 128:]
        kn = k * lax.rsqrt(_dot(k * k, b1_ref[...], prec=HI) + EPS) * kw_ref[...]
        kr = kn * cos_ + _rot(kn) * sin_
        k4_ref[...] = _bdot(kr, dup_ref[...]).astype(BF16)
        v4_ref[...] = _bdot(v, dup_ref[...]).astype(BF16)

    row = lambda w, cb: pl.BlockSpec((TM, w), lambda i: (i, cb))
    out = jax.ShapeDtypeStruct((tt, ATW), BF16)
    return _pcall(body, name="qk_prep", grid=(tt // TM,),
                  in_specs=[row(ATW, C_QRAW), row(256, C_KV), row(128, 0), row(128, 0),
                            _full((1, ATW)), _full((1, 128)), _full((ATW, ATW)), _full((128, 128)),
                            _full((128, ATW))],
                  out_specs=[row(ATW, 0)] * 3, out_shape=[out] * 3)(
                      p, p, cos, sin, qnw8, knw2, bd512, bd128, dup)


def _attn_masks(i, nb):
    r = lax.broadcasted_iota(jnp.int32, (BLK, 3 * BLK + L), 0)
    c = lax.broadcasted_iota(jnp.int32, (BLK, 3 * BLK + L), 1)
    kpos = (i - 1) * BLK + c
    loc = (jnp.abs(c - BLK - r) <= BLK) & (kpos >= 0) & (kpos < nb * BLK)
    return loc | (c >= 3 * BLK)


def _lane_mask(j):
    lane = lax.broadcasted_iota(jnp.int32, (1, 256), 1)
    return (lane // HDIM) == j


def _attn_specs(nb):
    blk = lambda off: pl.BlockSpec((BLK, ATW), lambda i: (jnp.clip(i + off, 0, nb - 1) + 2, 0))
    ctx = pl.BlockSpec((L, ATW), lambda i: (0, 0))
    return blk, ctx


def _attn_fwd(qr, k4, v4, sinks):
    tt = qr.shape[0]
    s_len = tt - L
    nb = s_len // BLK

    def body(sk_ref, q_ref, kp, ko, kn, kc, vp, vo, vn, vc, y_ref, lse_ref):
        i = pl.program_id(0)
        valid = _attn_masks(i, nb)
        q = q_ref[...]
        ys, lses = [], []
        for g in range(2):
            gs = slice(256 * g, 256 * g + 256)
            kcat = jnp.concatenate([kp[:, gs], ko[:, gs], kn[:, gs], kc[:, gs]], axis=0)
            vcat = jnp.concatenate([vp[:, gs], vo[:, gs], vn[:, gs], vc[:, gs]], axis=0)
            qg = q[:, gs]
            og = jnp.zeros((BLK, 256), F32)
            lg = jnp.zeros((BLK, 256), F32)
            for j in range(4):
                lm = _lane_mask(j)
                sink = sk_ref[4 * g + j]
                s = jnp.where(valid, _dot(jnp.where(lm, qg, jnp.zeros_like(qg)), kcat, NT), -1e30)
                m = jnp.maximum(jnp.max(s, axis=-1, keepdims=True), sink)
                e = jnp.exp(s - m)
                den = jnp.sum(e, axis=-1, keepdims=True) + jnp.exp(sink - m)
                pr = e / den
                og = og + jnp.where(lm, _bdot(pr, vcat), 0.0)
                lg = lg + jnp.where(lm, m + jnp.log(den), 0.0)
            ys.append(og)
            lses.append(lg)
        y_ref[...] = jnp.concatenate(ys, axis=1).astype(BF16)
        lse_ref[...] = jnp.concatenate(lses, axis=1)

    blk, ctx = _attn_specs(nb)
    out = pl.BlockSpec((BLK, ATW), lambda i: (i, 0))
    return _pcall(body, name="attn_fwd", grid=(nb,),
                  in_specs=[pl.BlockSpec(memory_space=pltpu.SMEM), blk(0),
                            blk(-1), blk(0), blk(1), ctx, blk(-1), blk(0), blk(1), ctx],
                  out_specs=[out, out],
                  out_shape=[jax.ShapeDtypeStruct((s_len, ATW), BF16),
                             jax.ShapeDtypeStruct((s_len, ATW), F32)])(
                      sinks, qr, k4, k4, k4, k4, v4, v4, v4, v4)


def _attn_bwd(qr, k4, v4, sinks, y, lse, dy):
    tt = qr.shape[0]
    s_len = tt - L
    nb = s_len // BLK

    def body(sk_ref, q_ref, kp, ko, kn, kc, vp, vo, vn, vc, y_ref, lse_ref, dy_ref,
             dq_ref, dkw_ref, dvw_ref, dkc_ref, dvc_ref, dsk_ref):
        i = pl.program_id(0)

        @pl.when(i == 0)
        def _():
            dkc_ref[...] = jnp.zeros_like(dkc_ref)
            dvc_ref[...] = jnp.zeros_like(dvc_ref)
            dsk_ref[...] = jnp.zeros_like(dsk_ref)

        valid = _attn_masks(i, nb)
        q = q_ref[...]
        dy_ = dy_ref[...]
        dly = dy_ * y_ref[...].astype(F32)
        lse_ = lse_ref[...]
        dqs = []
        for g in range(2):
            gs = slice(256 * g, 256 * g + 256)
            kcat = jnp.concatenate([kp[:, gs], ko[:, gs], kn[:, gs], kc[:, gs]], axis=0)
            vcat = jnp.concatenate([vp[:, gs], vo[:, gs], vn[:, gs], vc[:, gs]], axis=0)
            qg, dyg, dlg, lsg = q[:, gs], dy_[:, gs], dly[:, gs], lse_[:, gs]
            dqg = jnp.zeros((BLK, 256), F32)
            dkg = jnp.zeros((3 * BLK + L, 256), F32)
            dvg = jnp.zeros((3 * BLK + L, 256), F32)
            for j in range(4):
                lm = _lane_mask(j)
                sink = sk_ref[4 * g + j]
                qm = jnp.where(lm, qg, jnp.zeros_like(qg))
                dym = jnp.where(lm, dyg, 0.0).astype(BF16)
                lse_h = jnp.max(jnp.where(lm, lsg, -1e30), axis=-1, keepdims=True)
                delta = jnp.sum(jnp.where(lm, dlg, 0.0), axis=-1, keepdims=True)
                s = _dot(qm, kcat, NT)
                pr = jnp.where(valid, jnp.exp(s - lse_h), 0.0)
                dpr = _dot(dym, vcat, NT)
                dsc = pr * (dpr - delta)
                psink = jnp.exp(sink - lse_h)
                dsk_ref[4 * g + j:4 * g + j + 1, :] += jnp.broadcast_to(
                    -jnp.sum(psink * delta, axis=0, keepdims=True), (1, 128))
                dsb = dsc.astype(BF16)
                dqg = dqg + jnp.where(lm, _dot(dsb, kcat), 0.0)
                dkg = dkg + _dot(dsb, qm, TN)
                dvg = dvg + _dot(pr.astype(BF16), dym, TN)
            dqs.append(dqg)
            dkw_ref[0, :, gs] = dkg[:3 * BLK]
            dvw_ref[0, :, gs] = dvg[:3 * BLK]
            dkc_ref[:, gs] += dkg[3 * BLK:]
            dvc_ref[:, gs] += dvg[3 * BLK:]
        dq_ref[...] = jnp.concatenate(dqs, axis=1)

    blk, ctx = _attn_specs(nb)
    out = pl.BlockSpec((BLK, ATW), lambda i: (i, 0))
    win = pl.BlockSpec((1, 3 * BLK, ATW), lambda i: (i, 0, 0))
    acc = _full((L, ATW))
    return _pcall(body, name="attn_bwd", grid=(nb,),
                  in_specs=[pl.BlockSpec(memory_space=pltpu.SMEM), blk(0),
                            blk(-1), blk(0), blk(1), ctx, blk(-1), blk(0), blk(1), ctx, out, out, out],
                  out_specs=[out, win, win, acc, acc, _full((8, 128))],
                  out_shape=[jax.ShapeDtypeStruct((s_len, ATW), F32),
                             jax.ShapeDtypeStruct((nb, 3 * BLK, ATW), F32),
                             jax.ShapeDtypeStruct((nb, 3 * BLK, ATW), F32),
                             jax.ShapeDtypeStruct((L, ATW), F32), jax.ShapeDtypeStruct((L, ATW), F32),
                             jax.ShapeDtypeStruct((8, 128), F32)])(
                      sinks, qr, k4, k4, k4, k4, v4, v4, v4, v4, y, lse, dy)


def _attn_post(p, cos, sin, qnw8, knw2, bd512, bd128, dupt, dq, dkw, dvw, dkc, dvc, dp):
    tt = p.shape[0]
    s_len = tt - L
    nb = s_len // BLK
    nctx = L // BLK

    def body(q_ref, kv_ref, cos_ref, sin_ref, qw_ref, kw_ref, b5_ref, b1_ref, dupt_ref,
             dq_ref, kwp, kwo, kwn, vwp, vwo, vwn, dkc_ref, dvc_ref, _dp_in,
             dp_ref, dqw_ref, dkw_ref):
        t = pl.program_id(0)
        j = t - nctx

        @pl.when(t == 0)
        def _():
            dqw_ref[...] = jnp.zeros_like(dqw_ref)
            dkw_ref[...] = jnp.zeros_like(dkw_ref)

        is_lat = t >= nctx
        cos_, sin_ = cos_ref[...], sin_ref[...]
        has_p = is_lat & (j >= 1)
        has_n = is_lat & (j <= nb - 2)
        dk4 = (jnp.where(is_lat, kwo[0], dkc_ref[...]) + jnp.where(has_p, kwp[0], 0.0)
               + jnp.where(has_n, kwn[0], 0.0))
        dv4 = (jnp.where(is_lat, vwo[0], dvc_ref[...]) + jnp.where(has_p, vwp[0], 0.0)
               + jnp.where(has_n, vwn[0], 0.0))
        dkr = _dot(dk4, dupt_ref[...], prec=HI)
        dv = _dot(dv4, dupt_ref[...], prec=HI)
        kv = kv_ref[...]
        k = kv[:, :128]
        kw = kw_ref[...]
        rk = lax.rsqrt(_dot(k * k, b1_ref[...], prec=HI) + EPS)
        xk = k * rk
        dkn = dkr * cos_ + _rot(dkr * sin_)
        dxk = dkn * kw
        dk = rk * (dxk - xk * _dot(dxk * xk, b1_ref[...], prec=HI))
        dkw_ref[...] += jnp.sum(dkn * xk, axis=0, keepdims=True)
        q = q_ref[...]
        qw = qw_ref[...]
        rq = lax.rsqrt(_dot(q * q, b5_ref[...], prec=HI) + EPS)
        xq = q * rq
        cos4 = jnp.concatenate([cos_] * 4, axis=1)
        sin4 = jnp.concatenate([sin_] * 4, axis=1)
        dqr = jnp.where(is_lat, dq_ref[...], 0.0) * (HDIM ** -0.5)
        dqn = dqr * cos4 + _rot(dqr * sin4)
        dxq = dqn * qw
        dqraw = rq * (dxq - xq * _dot(dxq * xq, b5_ref[...], prec=HI))
        dqw_ref[...] += jnp.sum(dqn * xq, axis=0, keepdims=True)
        dp_ref[...] = jnp.concatenate([dqraw, dk, dv], axis=1).astype(BF16)

    row = lambda w, cb: pl.BlockSpec((BLK, w), lambda t: (t, cb))
    lat = pl.BlockSpec((BLK, ATW), lambda t: (jnp.maximum(t - nctx, 0), 0))

    def part(off):
        return pl.BlockSpec((1, BLK, ATW), lambda t: (jnp.clip(t - nctx + off, 0, nb - 1), 1 - off, 0))

    cacc = pl.BlockSpec((BLK, ATW), lambda t: (jnp.minimum(t, nctx - 1), 0))
    return _pcall(body, name="attn_post", grid=(tt // BLK,),
                  in_specs=[row(ATW, C_QRAW), row(256, C_KV), row(128, 0), row(128, 0),
                            _full((1, ATW)), _full((1, 128)), _full((ATW, ATW)), _full((128, 128)),
                            _full((ATW, 128)), lat, part(-1), part(0), part(1), part(-1), part(0), part(1),
                            cacc, cacc, ANY],
                  out_specs=[pl.BlockSpec((BLK, 768), lambda t: (t, C_QKV)), _full((1, ATW)), _full((1, 128))],
                  out_shape=[jax.ShapeDtypeStruct(dp.shape, BF16), jax.ShapeDtypeStruct((1, ATW), F32),
                             jax.ShapeDtypeStruct((1, 128), F32)],
                  aliases={18: 0})(p, p, cos, sin, qnw8, knw2, bd512, bd128, dupt,
                                   dq, dkw, dkw, dkw, dvw, dvw, dvw, dkc, dvc, dp)


def _merge(ah, aa, p):
    s_len = ah.shape[0]

    def body(ah_ref, aa_ref, gh_ref, ga_ref, m_ref):
        m_ref[...] = (_sig(gh_ref[...]) * ah_ref[...] + _sig(ga_ref[...]) * aa_ref[...]).astype(BF16)

    row = pl.BlockSpec((TM, D), lambda i: (i, 0))
    return _pcall(body, name="merge", grid=(s_len // TM,),
                  in_specs=[row, row, pl.BlockSpec((TM, D), lambda i: (i + 1, 2)),
                            pl.BlockSpec((TM, D), lambda i: (i + 1, 3))],
                  out_specs=row, out_shape=jax.ShapeDtypeStruct((s_len, D), BF16))(ah, aa, p, p)


def _merge_bwd(dm, ah, aa, p):
    tt = p.shape[0]
    s_len = tt - L

    def body(dm_ref, ah_ref, aa_ref, gh_ref, ga_ref, dp_ref, dmh_ref, dma_ref):
        i = pl.program_id(0)

        @pl.when(i == 0)
        def _():
            dp_ref[...] = jnp.zeros_like(dp_ref)

        @pl.when(i >= 1)
        def _():
            dm_ = dm_ref[...]
            sh, sa = _sig(gh_ref[...]), _sig(ga_ref[...])
            dp_ref[...] = jnp.concatenate([dm_ * ah_ref[...] * sh * (1.0 - sh),
                                           dm_ * aa_ref[...] * sa * (1.0 - sa)], axis=1).astype(BF16)
            dmh_ref[...] = (dm_ * sh).astype(BF16)
            dma_ref[...] = (dm_ * sa).astype(BF16)

    lat = pl.BlockSpec((TM, D), lambda i: (jnp.maximum(i - 1, 0), 0))
    return _pcall(body, name="merge_bwd", grid=(tt // TM,),
                  in_specs=[lat, lat, lat, pl.BlockSpec((TM, D), lambda i: (i, 2)),
                            pl.BlockSpec((TM, D), lambda i: (i, 3))],
                  out_specs=[pl.BlockSpec((TM, 2 * D), lambda i: (i, C_GATES)), lat, lat],
                  out_shape=[jax.ShapeDtypeStruct((tt, NCOL), BF16), jax.ShapeDtypeStruct((s_len, D), BF16),
                             jax.ShapeDtypeStruct((s_len, D), BF16)])(dm, ah, aa, p, p)


def _local_step(x, ctx, tgt, mod, modc, nw1, nw2, lg, hw, qnw, knw, sinks,
                w_in, bh4, ba4, w_o, g4, u4, dn4):
    s_len = x.shape[0]
    tt = s_len + L
    tok = jnp.concatenate([ctx, x], axis=0)
    ss1 = jnp.stack([modc, mod[0:2]])
    ss2 = mod[3:5][None]
    g1, g2 = mod[2:3], mod[5:6]
    hw4 = jnp.tile(hw, (1, 4))
    qnw8 = jnp.tile(qnw, (1, 8))
    knw2 = jnp.tile(knw, (1, 2))
    cos, sin = _rope_tables(s_len)
    bd512, bd128 = _blockdiag(ATW, HDIM), _blockdiag(128, HDIM)
    dupm = _dup_matrix()
    dup, dupt = jnp.asarray(dupm, BF16), jnp.asarray(dupm.T, F32)
    tmt = tt // 2 if tt % 512 else 512

    h = _modulate(tok, nw1, ss1, name="mod1", sel=lambda i: jnp.minimum(i, 1))
    p = _mm(h, w_in, name="mm_in", tm=tmt, tn=768, tk=D)
    o0, st0 = _hgrn_fwd(p, lg, rev=False)
    o1, st1 = _hgrn_fwd(p, lg, rev=True)
    y_hg = _readout(o0, o1, p, hw4)
    qr, k4, v4 = _qk_prep(p, cos, sin, qnw8, knw2, bd512, bd128, dup)
    y_at, lse = _attn_fwd(qr, k4, v4, sinks)
    ah = _mm_cs(y_hg, bh4, name="mm_bh")
    aa = _mm_cs(y_at, ba4, name="mm_ba")
    mixed = _merge(ah, aa, p)
    ao = _mm(mixed, w_o, name="mm_o", tm=512, tn=D, tk=D)
    x1, h2 = _res1_mod2(x, ao, g1, nw2, ss2)
    a4, b4, z4 = _ffn_up(h2, g4, u4)
    y = _ffn_down(z4, dn4)
    sq, dx2, dyb, dg2 = _loss_head(x1, y, g2, tgt)

    da4, db4 = _ffn_dz(dyb, dn4, a4, b4)
    g_dn = _ffn_gdn(z4, dyb)
    dh2 = _ffn_dh2(da4, db4, g4, u4)
    g_g, g_u = _ffn_ggu(h2, da4, db4)
    dx1, dattn, dss2, dnw2, dg1 = _mod2_bwd(x1, dh2, dx2, ao, nw2, ss2, g1)
    dm = _mm(dattn, w_o, name="mm_dm", mode="nt", tm=512, tn=D, tk=D)
    g_o = _mm(mixed, dattn, name="mm_go", mode="tn", tm=D, tn=D, tk=512)
    dp, dmh, dma = _merge_bwd(dm, ah, aa, p)
    dy_hg = _mm_cs_nt(dmh, bh4, name="mm_dyh")
    dy_at = _mm_cs_nt(dma, ba4, name="mm_dya")
    g_bh = _mm_cs_tn(y_hg, dmh, D // 4, name="mm_gbh")
    g_ba = _mm_cs_tn(y_at, dma, D // 4, name="mm_gba")
    dp, do, dhw4 = _readout_bwd(o0, o1, p, hw4, dy_hg, dp)
    dq, dkw, dvw, dkc, dvc, dsk = _attn_bwd(qr, k4, v4, sinks, y_at, lse, dy_at)
    dp, dqnw8, dknw2 = _attn_post(p, cos, sin, qnw8, knw2, bd512, bd128, dupt, dq, dkw, dvw, dkc, dvc, dp)
    dp, dv0, dq0, dlg0 = _hgrn_bwd(p, lg, do, st0, dp, None, rev=False)
    dp, dlg1 = _hgrn_bwd(p, lg, do, st1, dp, (dv0, dq0), rev=True)
    dh = _mm(dp, w_in, name="mm_dh", mode="nt", tm=tmt, tn=D, tk=768)
    g_in = _mm(h, dp, name="mm_gin", mode="tn", tm=D, tn=768, tk=tmt)
    gx, dss1, dnw1 = _mod1_bwd(tok, dh, dx1, nw1, ss1)

    dmod = jnp.concatenate([dss1[1], dg1, dss2, dg2], axis=0)
    dmodc = dss1[0]
    raw = (dss1, dg1, dss2, dg2, dnw1, dnw2, dhw4, dqnw8, dknw2, dsk, dlg0, dlg1)
    small = dict(raw=raw, dmod=dmod, dmodc=dmodc, dnw1=dnw1, dnw2=dnw2,
                 dhw=dhw4.reshape(4, HGD).sum(0, keepdims=True),
                 dqnw=dqnw8.reshape(8, HDIM).sum(0, keepdims=True),
                 dknw=dknw2.reshape(2, HDIM).sum(0, keepdims=True),
                 dsinks=dsk[:, 0], dlg=jnp.concatenate([dlg0, dlg1], axis=0))
    big = dict(w_in=g_in, w_bh=g_bh, w_ba=g_ba, w_o=g_o, w_g=g_g, w_u=g_u, w_dn=g_dn)
    return sq, gx, big, small


def _to_kernel_cols(w):
    return jnp.concatenate([w[..., 512:1024], w[..., 1024:1536], w[..., 1792:2304], w[..., 0:512],
                            w[..., 3328:5376], w[..., 2304:2816], w[..., 2816:3328], w[..., 1536:1792]], axis=-1)


def _from_kernel_cols(g):
    return jnp.concatenate([g[..., 1536:2048], g[..., 0:512], g[..., 512:1024], g[..., 5120:5376],
                            g[..., 1024:1536], g[..., 4096:4608], g[..., 4608:5120], g[..., 2048:4096]], axis=-1)


def _place():
    x, y, c = lax.axis_index("x"), lax.axis_index("y"), lax.axis_index("c")
    return x, y, c


def _gather_blocks(x_refs, out_refs, send_sems, recv_sems, local_sems):
    n = len(out_refs)
    x, y, c = _place()
    me, sibling = (x, y, c), (x, y, 1 - c)
    chips = [(1 - x, y), (x, 1 - y), (1 - x, 1 - y)]

    def slot(u, px, py, pc):
        return out_refs[u].at[4 * px + 2 * py + pc]

    def copy(u, k, block, to, src=None):
        return pltpu.make_async_remote_copy(
            src_ref=slot(u, *block) if src is None else src, dst_ref=slot(u, *block),
            send_sem=send_sems.at[u, k], recv_sem=recv_sems.at[u, k], device_id=to, device_id_type=MESH)

    mines = []
    if x_refs is not None:
        mines = [pltpu.make_async_copy(x_refs[u], slot(u, *me), local_sems.at[u]) for u in range(n)]
    for cp in mines:
        cp.start()
    first = []
    for u in range(n):
        src = None if x_refs is None else x_refs[u]
        first.append(copy(u, 0, me, sibling, src=src))
        first += [copy(u, 1 + j, me, (*chip, c), src=src) for j, chip in enumerate(chips)]
    for cp in first:
        cp.start()
    passed = []
    for j, chip in enumerate(chips):
        for u in range(n):
            copy(u, 1 + j, (*chip, c), me).wait_recv()
            fwd = copy(u, 4 + j, (*chip, c), sibling)
            fwd.start()
            passed.append(fwd)
    for u in range(n):
        copy(u, 0, sibling, me).wait_recv()
    for j, chip in enumerate(chips):
        for u in range(n):
            copy(u, 4 + j, (*chip, 1 - c), me).wait_recv()
    for cp in first + passed:
        cp.wait_send()
    for cp in mines:
        cp.wait()


def _gather_sems(n):
    return [pltpu.SemaphoreType.DMA((n, 7)), pltpu.SemaphoreType.DMA((n, 7)), pltpu.SemaphoreType.DMA((n,))]


def _allgather(blks, *, name, in_vmem):
    n = len(blks)
    space = pltpu.VMEM if in_vmem else pl.ANY

    def body(*refs):
        _gather_blocks(refs[:n], refs[n:2 * n], *refs[2 * n:])

    return pl.pallas_call(
        body, name=name, out_shape=[jax.ShapeDtypeStruct((8,) + b.shape, b.dtype) for b in blks],
        in_specs=[pl.BlockSpec(memory_space=space)] * n, out_specs=[pl.BlockSpec(memory_space=space)] * n,
        scratch_shapes=_gather_sems(n))(*blks)


def _cast_place(ws, c, dev):
    n = len(ws)

    def body(s_ref, *refs):
        for u in range(n):
            refs[n + u][0] = refs[u][...].astype(BF16)

    in_specs, out_specs, out_shape = [], [], []
    for w in ws:
        q, cols = w.shape[0] // 4, w.shape[1]
        in_specs.append(pl.BlockSpec((q, cols), lambda i, s: (2 * s[0] + i, 0)))
        out_specs.append(pl.BlockSpec((1, q, cols), lambda i, s: (s[1], i, 0)))
        out_shape.append(jax.ShapeDtypeStruct((8, 2 * q, cols), BF16))
    return pl.pallas_call(
        body, name="cast_place",
        grid_spec=pltpu.PrefetchScalarGridSpec(num_scalar_prefetch=1, grid=(2,), in_specs=in_specs,
                                               out_specs=out_specs),
        out_shape=out_shape,
        compiler_params=pltpu.CompilerParams(vmem_limit_bytes=48 << 20))(jnp.stack([c, dev]), *ws)


def _allgather_inplace(bufs, *, name):
    n = len(bufs)

    def body(*refs):
        _gather_blocks(None, refs[n:2 * n], *refs[2 * n:])

    return pl.pallas_call(
        body, name=name, out_shape=[jax.ShapeDtypeStruct(b.shape, b.dtype) for b in bufs],
        in_specs=[ANY] * n, out_specs=[ANY] * n, input_output_aliases={u: u for u in range(n)},
        scratch_shapes=_gather_sems(n))(*bufs)


def _ag_small(raw):
    def body(dss1, dg1, dss2, dg2, dnw1, dnw2, dhw4, dqnw8, dknw2, dsk, dlg0, dlg1,
             out_ref, tot_ref, blk, send_sems, recv_sems, local_sems):
        blk[...] = jnp.zeros_like(blk)
        blk[0:2, :] = dss1[1]
        blk[2:3, :] = dg1[...]
        blk[3:5, :] = dss2[...]
        blk[5:6, :] = dg2[...]
        blk[6:8, :] = dss1[0]
        blk[8:9, :] = dnw1[...]
        blk[9:10, :] = dnw2[...]
        blk[10:11, 0:HGW] = dhw4[...]
        blk[10:11, HGW:D] = dqnw8[...]
        blk[11:12, 0:128] = dknw2[...]
        blk[12:14, 0:HGW] = dlg0[0]
        blk[14:16, 0:HGW] = dlg1[0]
        blk[16:24, 0:128] = dsk[...]
        _gather_blocks([blk], [out_ref], send_sems, recv_sems, local_sems)
        acc = out_ref[0]
        for i in range(1, 8):
            acc = acc + out_ref[i]
        tot_ref[...] = acc

    vm = pl.BlockSpec(memory_space=pltpu.VMEM)
    return pl.pallas_call(
        body, name="ag_small",
        out_shape=[jax.ShapeDtypeStruct((8, 24, D), F32), jax.ShapeDtypeStruct((24, D), F32)],
        in_specs=[vm] * 12, out_specs=[vm, vm],
        scratch_shapes=[pltpu.VMEM((24, D), F32)] + _gather_sems(1))(*raw)


def _rs_pair_exchange(units):
    n = len(units)

    def body(*refs):
        g_refs, r_refs = refs[:n], refs[n:2 * n]
        send_sems, recv_sems = refs[2 * n:]
        x, y, c = _place()
        cps = [pltpu.make_async_remote_copy(
            src_ref=g_refs[u].at[j, 1 - c], dst_ref=r_refs[u].at[j], send_sem=send_sems.at[u, j],
            recv_sem=recv_sems.at[u, j], device_id=(x, y, 1 - c), device_id_type=MESH)
            for u in range(n) for j in range(4)]
        for cp in cps:
            cp.start()
        for cp in cps:
            cp.wait()

    return pl.pallas_call(
        body, name="rs_pair_exchange",
        out_shape=[jax.ShapeDtypeStruct((4,) + g.shape[2:], g.dtype) for g in units],
        in_specs=[ANY] * n, out_specs=[ANY] * n,
        scratch_shapes=[pltpu.SemaphoreType.DMA((n, 4)), pltpu.SemaphoreType.DMA((n, 4))])(*units)


def _rs_pair_add(units, recvs, c):
    n = len(units)

    def body(c_ref, *refs):
        for u in range(n):
            refs[2 * n + u][...] = (refs[u][0] + refs[n + u][...]).astype(BF16)

    in_specs, out_specs, out_shape = [], [], []
    for g in units:
        h, w = g.shape[2] // 2, g.shape[3]
        in_specs.append(pl.BlockSpec((1, 1, h, w), lambda j, i, cr: (j, cr[0], i, 0)))
    for g in units:
        h, w = g.shape[2] // 2, g.shape[3]
        in_specs.append(pl.BlockSpec((1, h, w), lambda j, i, cr: (j, i, 0)))
        out_specs.append(pl.BlockSpec((1, h, w), lambda j, i, cr: (j, i, 0)))
        out_shape.append(jax.ShapeDtypeStruct((4, 2 * h, w), BF16))
    return pl.pallas_call(
        body, name="rs_pair_add",
        grid_spec=pltpu.PrefetchScalarGridSpec(num_scalar_prefetch=1, grid=(4, 2), in_specs=in_specs,
                                               out_specs=out_specs),
        out_shape=out_shape,
        compiler_params=pltpu.CompilerParams(vmem_limit_bytes=48 << 20))(c.reshape(1), *units, *recvs)


def _rs_chip_exchange(pairs):
    n = len(pairs)

    def body(*refs):
        p_refs, r_refs = refs[:n], refs[n:2 * n]
        send_sems, recv_sems = refs[2 * n:]
        x, y, c = _place()
        k = 2 * x + y
        sends = []
        for d in range(1, 4):
            j = (k + d) % 4
            for u in range(n):
                sends.append(pltpu.make_async_remote_copy(
                    src_ref=p_refs[u].at[j], dst_ref=r_refs[u].at[k], send_sem=send_sems.at[u, d - 1],
                    recv_sem=recv_sems.at[u, d - 1], device_id=(j // 2, j % 2, c), device_id_type=MESH))
        for cp in sends:
            cp.start()
        for d in range(1, 4):
            src = (k + 4 - d) % 4
            for u in range(n):
                pltpu.make_async_remote_copy(
                    src_ref=p_refs[u].at[src], dst_ref=r_refs[u].at[src], send_sem=send_sems.at[u, d - 1],
                    recv_sem=recv_sems.at[u, d - 1], device_id=(x, y, c), device_id_type=MESH).wait_recv()
        for cp in sends:
            cp.wait_send()

    return pl.pallas_call(
        body, name="rs_chip_exchange", out_shape=[jax.ShapeDtypeStruct(p.shape, p.dtype) for p in pairs],
        in_specs=[ANY] * n, out_specs=[ANY] * n,
        scratch_shapes=[pltpu.SemaphoreType.DMA((n, 3)), pltpu.SemaphoreType.DMA((n, 3))])(*pairs)


def _rs_chip_add(pairs, contribs, c, chip):
    n = len(pairs)

    def body(s_ref, *refs):
        for u in range(n):
            a, b, c_, d = refs[4 * u:4 * u + 4]
            refs[4 * n + u][0] = ((a[0].astype(F32) + b[0].astype(F32)) + c_[0].astype(F32)) + d[0].astype(F32)

    in_specs, out_specs, out_shape, args = [], [], [], []
    for p, r in zip(pairs, contribs):
        h, w = p.shape[1] // 2, p.shape[2]
        in_specs += [pl.BlockSpec((1, h, w), functools.partial(lambda d, i, s: ((s[1] + d) % 4, i, 0), d))
                     for d in range(4)]
        args += [p, r, r, r]
        out_specs.append(pl.BlockSpec((1, h, w), lambda i, s: (s[0], i, 0)))
        out_shape.append(jax.ShapeDtypeStruct((2, 2 * h, w), F32))
    return pl.pallas_call(
        body, name="rs_chip_add",
        grid_spec=pltpu.PrefetchScalarGridSpec(num_scalar_prefetch=1, grid=(2,), in_specs=in_specs,
                                               out_specs=out_specs),
        out_shape=out_shape,
        compiler_params=pltpu.CompilerParams(vmem_limit_bytes=48 << 20))(jnp.stack([c, chip]), *args)


def _rs_sibling_gather(reds):
    n = len(reds)

    def body(*refs):
        o_refs = refs[n:2 * n]
        send_sems, recv_sems = refs[2 * n:]
        x, y, c = _place()
        cps = [pltpu.make_async_remote_copy(
            src_ref=o_refs[u].at[c], dst_ref=o_refs[u].at[c], send_sem=send_sems.at[u], recv_sem=recv_sems.at[u],
            device_id=(x, y, 1 - c), device_id_type=MESH) for u in range(n)]
        for cp in cps:
            cp.start()
        for u in range(n):
            cps[u].wait_send()
            pltpu.make_async_remote_copy(
                src_ref=o_refs[u].at[1 - c], dst_ref=o_refs[u].at[1 - c], send_sem=send_sems.at[u],
                recv_sem=recv_sems.at[u], device_id=(x, y, 1 - c), device_id_type=MESH).wait_recv()

    return pl.pallas_call(
        body, name="rs_sibling_gather", out_shape=[jax.ShapeDtypeStruct(r.shape, r.dtype) for r in reds],
        in_specs=[ANY] * n, out_specs=[ANY] * n, input_output_aliases={u: u for u in range(n)},
        scratch_shapes=[pltpu.SemaphoreType.DMA((n,))] * 2)(*reds)


def _ada_fwd(c16, w, b):
    n = w.shape[1]
    tn = 512

    def body(c_ref, w_ref, b_ref, o_ref):
        cc = c_ref[...]
        o_ref[...] = _dot(cc * _sig(cc), w_ref[...], prec=HI) + b_ref[...]

    return _pcall(body, name="ada_fwd", grid=(n // tn,),
                  in_specs=[_full((16, D)), pl.BlockSpec((D, tn), lambda j: (0, j)),
                            pl.BlockSpec((1, tn), lambda j: (0, j))],
                  out_specs=pl.BlockSpec((16, tn), lambda j: (0, j)),
                  out_shape=jax.ShapeDtypeStruct((16, n), F32))(c16, w, b)


def _ada_bwd(c16, dmod16, w):
    n = w.shape[1]
    tn = 512

    def body(c_ref, d_ref, w_ref, gw_ref, gc_ref):
        j = pl.program_id(0)

        @pl.when(j == 0)
        def _():
            gc_ref[...] = jnp.zeros_like(gc_ref)

        cc = c_ref[...]
        dm = d_ref[...]
        gw_ref[...] = _dot(cc * _sig(cc), dm, TN, prec=HI)
        gc_ref[...] += _dot(dm, w_ref[...], NT, prec=HI)

    return _pcall(body, name="ada_bwd", grid=(n // tn,),
                  in_specs=[_full((16, D)), pl.BlockSpec((16, tn), lambda j: (0, j)),
                            pl.BlockSpec((D, tn), lambda j: (0, j))],
                  out_specs=[pl.BlockSpec((D, tn), lambda j: (0, j)), _full((16, D))],
                  out_shape=[jax.ShapeDtypeStruct((D, n), F32),
                             jax.ShapeDtypeStruct((16, D), F32)])(c16, dmod16, w)


def _adam_math(w, g, m, v):
    c1 = 1.0 - ADAM_B1 ** ADAM_STEP
    c2 = 1.0 - ADAM_B2 ** ADAM_STEP
    nm = ADAM_B1 * m + (1.0 - ADAM_B1) * g
    nv = ADAM_B2 * v + (1.0 - ADAM_B2) * (g * g)
    return -ADAM_LR * ((nm / c1) / (jnp.sqrt(nv / c2) + ADAM_EPS) + ADAM_WD * w), nm, nv


def _adamw_small(ws, gs, ms, vs):
    n = len(ws)

    def body(*refs):
        for u in range(n):
            d_, nm, nv = _adam_math(refs[u][...], refs[n + u][...], refs[2 * n + u][...], refs[3 * n + u][...])
            refs[4 * n + u][...] = d_
            refs[5 * n + u][...] = nm
            refs[6 * n + u][...] = nv

    specs = [_full(w.shape) for w in ws]
    shapes = [jax.ShapeDtypeStruct(w.shape, F32) for w in ws]
    out = _pcall(body, name="adamw_small", grid=(1,), in_specs=specs * 4, out_specs=specs * 3,
                 out_shape=shapes * 3)(*ws, *gs, *ms, *vs)
    return out[:n], out[n:2 * n], out[2 * n:]


def _cctx_grad(parts, c_ctx):
    def body(p_ref, c_ref, o_ref):
        acc = p_ref[0:1, :]
        for k in range(1, 4):
            acc = acc + p_ref[k:k + 1, :]
        cc = c_ref[...]
        s = _sig(cc)
        o_ref[...] = acc * (s * (1.0 + cc * (1.0 - s)))

    return _pcall(body, name="cctx_grad", grid=(1,), in_specs=[_full(parts.shape), _full((1, D))],
                  out_specs=_full((1, D)), out_shape=jax.ShapeDtypeStruct((1, D), F32))(parts, c_ctx)


def _adamw(w, g, m, v, *, name):
    rows, cols = w.shape
    tr = next((t for t in (256, 128, 64) if rows % t == 0), rows)

    def body(w_ref, g_ref, m_ref, v_ref, d_ref, nm_ref, nv_ref):
        d_ref[...], nm_ref[...], nv_ref[...] = _adam_math(w_ref[...], g_ref[...], m_ref[...], v_ref[...])

    spec = pl.BlockSpec((tr, cols), lambda i: (i, 0))
    out = jax.ShapeDtypeStruct((rows, cols), F32)
    return _pcall(body, name=name, grid=(rows // tr,), in_specs=[spec] * 4, out_specs=[spec] * 3,
                  out_shape=[out] * 3)(w, g, m, v)


def kernel(x, c, ctx, c_ctx, w_ada, b_ada, norm_mix_w, norm_ffn_w, w_in, hgrn_lb_logits, hgrn_norm_w, q_norm_w, k_norm_w, attn_sinks, w_branch_hgrn, w_branch_attn, w_out, w_ffn_gate, w_ffn_up, w_ffn_down, loss_target, m_c_ctx, m_w_ada, m_b_ada, m_norm_mix_w, m_norm_ffn_w, m_w_in, m_hgrn_lb_logits, m_hgrn_norm_w, m_q_norm_w, m_k_norm_w, m_attn_sinks, m_w_branch_hgrn, m_w_branch_attn, m_w_out, m_w_ffn_gate, m_w_ffn_up, m_w_ffn_down, v_c_ctx, v_w_ada, v_b_ada, v_norm_mix_w, v_norm_ffn_w, v_w_in, v_hgrn_lb_logits, v_hgrn_norm_w, v_q_norm_w, v_k_norm_w, v_attn_sinks, v_w_branch_hgrn, v_w_branch_attn, v_w_out, v_w_ffn_gate, v_w_ffn_up, v_w_ffn_down):
    xi, yi, ci = _place()
    chip = 2 * xi + yi
    dev = 2 * chip + ci
    s_len = x.shape[1]

    lbrow = jnp.pad(hgrn_lb_logits.reshape(1, 512), ((0, 0), (0, D - 512)))
    blk = jnp.concatenate([c, lbrow, jnp.zeros((6, D), F32)], axis=0)
    g0, = _allgather([blk], name="ag_cond", in_vmem=True)
    c16 = jnp.concatenate([g0[:, 0], c_ctx[None], jnp.zeros((7, D), F32)], axis=0)
    lg = g0[0::2, 1, :512].reshape(4, 2, 2, 128).transpose(1, 2, 0, 3).reshape(2, 2, HGW)

    nada = w_ada.shape[2]
    b_sh = lax.dynamic_slice(b_ada, (0, chip * nada), (1, nada))
    mod_sh = _ada_fwd(c16, w_ada[0], b_sh)
    g1, = _allgather([mod_sh], name="ag_mod", in_vmem=True)
    modall = g1[0::2].transpose(1, 0, 2).reshape(16, 4 * nada)
    mod = lax.dynamic_slice(modall, (dev, 0), (1, 6 * D)).reshape(6, D)
    modc = modall[8].reshape(6, D)[:2]

    shards = [w_in[0], w_branch_hgrn[0], w_branch_attn[0], w_out[0], w_ffn_gate[0], w_ffn_up[0], w_ffn_down[0]]
    gathered = _allgather_inplace(_cast_place(shards, ci, dev), name="ag_weights")
    in4, bh4, ba4, o4, g4, u4, dn4 = [g.reshape(4, 2 * g.shape[1], g.shape[2]) for g in gathered]
    w_in_k = _to_kernel_cols(in4.transpose(1, 0, 2).reshape(D, NCOL))

    sq, gx, big, small = _local_step(
        x[0], ctx[0], loss_target[0], mod, modc, norm_mix_w, norm_ffn_w, lg, hgrn_norm_w, q_norm_w,
        k_norm_w, attn_sinks[0], w_in_k, bh4, ba4, o4.reshape(D, D), g4, u4, dn4)
    loss = lax.psum(0.5 * jnp.sum(sq) / D, ("x", "y", "c"))

    gin4 = _from_kernel_cols(big["w_in"]).reshape(D, 4, NCOL // 4).transpose(1, 0, 2)
    units = [gin4, big["w_bh"], big["w_ba"], big["w_o"].reshape(4, D // 4, D), big["w_g"], big["w_u"], big["w_dn"]]
    units = [u.reshape(4, 2, u.shape[1] // 2, u.shape[2]) for u in units]
    recvs = _rs_pair_exchange(units)
    pairs = _rs_pair_add(units, recvs, ci)
    contribs = _rs_chip_exchange(pairs)
    reds = _rs_chip_add(pairs, contribs, ci, chip)
    gsh = [r.reshape(2 * r.shape[1], r.shape[2]) for r in _rs_sibling_gather(reds)]

    g2, tot = _ag_small(small["raw"])
    dmodc_tot = jnp.pad(tot[6:8].reshape(1, 2 * D), ((0, 0), (0, 4 * D)))
    g_b_ada = tot[0:6].reshape(1, 6 * D) + dmodc_tot
    dmod16 = jnp.concatenate([g2[:, 0:6].reshape(8, 6 * D), dmodc_tot, jnp.zeros((7, 6 * D), F32)], axis=0)
    g_w_ada, gc_part = _ada_bwd(c16, lax.dynamic_slice(dmod16, (0, chip * nada), (16, nada)), w_ada[0])
    g3, = _allgather([gc_part[8:16]], name="ag_cctx", in_vmem=True)
    g_c_ctx = _cctx_grad(g3[0::2, 0], c_ctx[None])[0]
    g_nw1 = tot[8:9]
    g_nw2 = tot[9:10]
    g_hw = tot[10, :HGW].reshape(4, HGD).sum(0, keepdims=True)
    g_qnw = tot[10, HGW:].reshape(8, HDIM).sum(0, keepdims=True)
    g_knw = tot[11, :128].reshape(2, HDIM).sum(0, keepdims=True)
    g_sinks = tot[16:24, 0][None]
    g_lg = lax.dynamic_slice(tot[12:16, :HGW].reshape(2, 2, HGW), (0, 0, chip * 128), (2, 2, 128))

    names = ["c_ctx", "w_ada", "b_ada", "norm_mix_w", "norm_ffn_w", "w_in", "hgrn_lb_logits", "hgrn_norm_w",
             "q_norm_w", "k_norm_w", "attn_sinks", "w_branch_hgrn", "w_branch_attn", "w_out", "w_ffn_gate",
             "w_ffn_up", "w_ffn_down"]
    ws = dict(zip(names, [c_ctx, w_ada, b_ada, norm_mix_w, norm_ffn_w, w_in, hgrn_lb_logits, hgrn_norm_w,
                          q_norm_w, k_norm_w, attn_sinks, w_branch_hgrn, w_branch_attn, w_out, w_ffn_gate,
                          w_ffn_up, w_ffn_down]))
    ms = dict(zip(names, [m_c_ctx, m_w_ada, m_b_ada, m_norm_mix_w, m_norm_ffn_w, m_w_in, m_hgrn_lb_logits,
                          m_hgrn_norm_w, m_q_norm_w, m_k_norm_w, m_attn_sinks, m_w_branch_hgrn,
                          m_w_branch_attn, m_w_out, m_w_ffn_gate, m_w_ffn_up, m_w_ffn_down]))
    vs = dict(zip(names, [v_c_ctx, v_w_ada, v_b_ada, v_norm_mix_w, v_norm_ffn_w, v_w_in, v_hgrn_lb_logits,
                          v_hgrn_norm_w, v_q_norm_w, v_k_norm_w, v_attn_sinks, v_w_branch_hgrn,
                          v_w_branch_attn, v_w_out, v_w_ffn_gate, v_w_ffn_up, v_w_ffn_down]))
    grads = dict(c_ctx=g_c_ctx, w_ada=g_w_ada[None], b_ada=g_b_ada, norm_mix_w=g_nw1, norm_ffn_w=g_nw2,
                 w_in=gsh[0][None], hgrn_lb_logits=g_lg, hgrn_norm_w=g_hw, q_norm_w=g_qnw, k_norm_w=g_knw,
                 attn_sinks=g_sinks, w_branch_hgrn=gsh[1][None], w_branch_attn=gsh[2][None], w_out=gsh[3][None],
                 w_ffn_gate=gsh[4][None], w_ffn_up=gsh[5][None], w_ffn_down=gsh[6][None])
    big_names = ["w_ada", "w_in", "w_branch_hgrn", "w_branch_attn", "w_out", "w_ffn_gate", "w_ffn_up", "w_ffn_down"]
    small_names = [n for n in names if n not in big_names]
    delta, new_m, new_v = {}, {}, {}
    for n in big_names:
        d_, m_, v_ = _adamw(ws[n][0], grads[n][0], ms[n][0], vs[n][0], name="adamw_" + n)
        delta[n], new_m[n], new_v[n] = d_[None], m_[None], v_[None]

    def two_d(a):
        return a.reshape(1, -1) if a.ndim == 1 else a

    sd, sm_, sv = _adamw_small(*[[two_d(d[n]) for n in small_names] for d in (ws, grads, ms, vs)])
    for i, n in enumerate(small_names):
        for dst, src in ((delta, sd), (new_m, sm_), (new_v, sv)):
            dst[n] = src[i].reshape(ws[n].shape)
    return (loss, gx[None], *[grads[n] for n in names], *[delta[n] for n in names],
            *[new_m[n] for n in names], *[new_v[n] for n in names])
```

```python
import functools

import numpy as np
import jax
import jax.numpy as jnp
from jax import lax
from jax.experimental import pallas as pl
from jax.experimental.pallas import tpu as pltpu

F32 = jnp.float32
BF16 = jnp.bfloat16
HI = lax.Precision.HIGHEST
MESH = pl.DeviceIdType.MESH

D = 1024
L = 256
TM = 256
HGW = 512
HGD = 128
CH = 32
ATW = 512
HDIM = 64
BLK = 128
GRID_W = 64
DFF = 2816
NCOL = 5376
EPS = 1e-6
ROPE_THETA = 10000.0

C_FB, C_INP, C_QHG, C_FF = 0, 1, 2, 3
C_GATES = 1
C_GHG, C_QRAW = 8, 9
C_KV = 20
C_QKV = 6

ADAM_LR, ADAM_B1, ADAM_B2, ADAM_EPS, ADAM_WD, ADAM_STEP = 0.001, 0.9, 0.999, 1e-08, 0.01, 10

NN = (((1,), (0,)), ((), ()))
NT = (((1,), (1,)), ((), ()))
TN = (((0,), (0,)), ((), ()))


def _dot(a, b, dims=NN, prec=None):
    return lax.dot_general(a, b, dims, precision=prec, preferred_element_type=F32)


def _bdot(a, b, dims=NN):
    return _dot(a.astype(BF16), b.astype(BF16), dims)


def _sig(x):
    return 1.0 / (1.0 + jnp.exp(-x))


def _pcall(body, *, name, grid, in_specs, out_specs, out_shape, scratch=(), aliases=None, vmem_mb=48):
    return pl.pallas_call(
        body, name=name, grid=grid, in_specs=in_specs, out_specs=out_specs, out_shape=out_shape,
        scratch_shapes=list(scratch), input_output_aliases=aliases or {},
        compiler_params=pltpu.CompilerParams(
            dimension_semantics=("arbitrary",) * len(grid), vmem_limit_bytes=vmem_mb << 20))


def _full(shape):
    nd = len(shape)
    return pl.BlockSpec(shape, lambda *_: (0,) * nd)


ANY = pl.BlockSpec(memory_space=pl.ANY)


def _mm(a, b, *, name, mode="nn", out_dtype=F32, tm, tn, tk):
    if mode == "nn":
        (m, k), (k2, n) = a.shape, b.shape
    elif mode == "nt":
        (m, k), (n, k2) = a.shape, b.shape
    else:
        (k, m), (k2, n) = a.shape, b.shape
    assert k == k2 and m % tm == 0 and n % tn == 0 and k % tk == 0, (name, a.shape, b.shape)
    nk = k // tk
    dims = {"nn": NN, "nt": NT, "tn": TN}[mode]

    def body(a_ref, b_ref, o_ref, acc):
        kk = pl.program_id(2)

        @pl.when(kk == 0)
        def _():
            acc[...] = jnp.zeros_like(acc)

        acc[...] += _bdot(a_ref[...], b_ref[...], dims)

        @pl.when(kk == nk - 1)
        def _():
            o_ref[...] = acc[...].astype(out_dtype)

    a_spec = (pl.BlockSpec((tk, tm), lambda i, j, kk: (kk, i)) if mode == "tn"
              else pl.BlockSpec((tm, tk), lambda i, j, kk: (i, kk)))
    b_spec = (pl.BlockSpec((tn, tk), lambda i, j, kk: (j, kk)) if mode == "nt"
              else pl.BlockSpec((tk, tn), lambda i, j, kk: (kk, j)))
    return _pcall(body, name=name, grid=(m // tm, n // tn, nk), in_specs=[a_spec, b_spec],
                  out_specs=pl.BlockSpec((tm, tn), lambda i, j, kk: (i, j)),
                  out_shape=jax.ShapeDtypeStruct((m, n), out_dtype),
                  scratch=[pltpu.VMEM((tm, tn), F32)])(a, b)


NT_IN = NCOL // 256


def _src_block(j):
    return j + jnp.where(j < 4, 2, jnp.where(j < 6, 3, jnp.where(j < 8, -6, jnp.where(
        j < 16, 5, jnp.where(j < 20, -7, -14)))))


def _mm_in(h, wt, tm):
    tt = h.shape[0]

    def body(h_ref, w_ref, o_ref):
        o_ref[...] = _bdot(h_ref[...], w_ref[...], NT)

    return _pcall(body, name="mm_in", grid=(tt // tm, NT_IN),
                  in_specs=[pl.BlockSpec((tm, D), lambda i, j: (i, 0)),
                            pl.BlockSpec((256, D), lambda i, j: (_src_block(j), 0))],
                  out_specs=pl.BlockSpec((tm, 256), lambda i, j: (i, j)),
                  out_shape=jax.ShapeDtypeStruct((tt, NCOL), F32))(h, wt)


def _mm_dh(dp, wt, tm):
    tt = dp.shape[0]

    def body(d_ref, w_ref, o_ref, acc):
        kk = pl.program_id(1)

        @pl.when(kk == 0)
        def _():
            acc[...] = jnp.zeros_like(acc)

        acc[...] += _bdot(d_ref[...], w_ref[...])

        @pl.when(kk == NT_IN - 1)
        def _():
            o_ref[...] = acc[...]

    return _pcall(body, name="mm_dh", grid=(tt // tm, NT_IN),
                  in_specs=[pl.BlockSpec((tm, 256), lambda i, kk: (i, kk)),
                            pl.BlockSpec((256, D), lambda i, kk: (_src_block(kk), 0))],
                  out_specs=pl.BlockSpec((tm, D), lambda i, kk: (i, 0)),
                  out_shape=jax.ShapeDtypeStruct((tt, D), F32), scratch=[pltpu.VMEM((tm, D), F32)])(dp, wt)


def _mm_gin(dp, h, tk):
    tt = dp.shape[0]
    nk = tt // tk

    def body(d_ref, h_ref, o_ref, acc):
        kk = pl.program_id(1)

        @pl.when(kk == 0)
        def _():
            acc[...] = jnp.zeros_like(acc)

        acc[...] += _bdot(d_ref[...], h_ref[...], TN)

        @pl.when(kk == nk - 1)
        def _():
            o_ref[...] = acc[...]

    return _pcall(body, name="mm_gin", grid=(NT_IN, nk),
                  in_specs=[pl.BlockSpec((tk, 256), lambda j, kk: (kk, j)),
                            pl.BlockSpec((tk, D), lambda j, kk: (kk, 0))],
                  out_specs=pl.BlockSpec((256, D), lambda j, kk: (_src_block(j), 0)),
                  out_shape=jax.ShapeDtypeStruct((NCOL, D), F32), scratch=[pltpu.VMEM((256, D), F32)])(dp, h)


def _modulate(xin, nw, ss, *, name, sel):
    rows = xin.shape[0]

    def body(x_ref, nw_ref, ss_ref, h_ref):
        x = x_ref[...]
        r = lax.rsqrt(jnp.mean(x * x, axis=-1, keepdims=True) + EPS)
        s = ss_ref[0]
        h_ref[...] = ((x * r * nw_ref[...]) * (1.0 + s[1:2]) + s[0:1]).astype(BF16)

    return _pcall(body, name=name, grid=(rows // TM,),
                  in_specs=[pl.BlockSpec((TM, D), lambda i: (i, 0)), _full((1, D)),
                            pl.BlockSpec((1, 2, D), lambda i: (sel(i), 0, 0))],
                  out_specs=pl.BlockSpec((TM, D), lambda i: (i, 0)),
                  out_shape=jax.ShapeDtypeStruct((rows, D), BF16))(xin, nw, ss)


def _norm_bwd_rows(x, dh, nw, scale):
    r = lax.rsqrt(jnp.mean(x * x, axis=-1, keepdims=True) + EPS)
    xh = x * r
    dxh = dh * ((1.0 + scale) * nw)
    dx = r * (dxh - xh * jnp.mean(dxh * xh, axis=-1, keepdims=True))
    return dx, xh


def _res1_mod2(x, ao, g1, nw2, ss2):
    s_len = x.shape[0]

    def body(x_ref, ao_ref, g_ref, nw_ref, ss_ref, x1_ref, h_ref):
        x1 = x_ref[...] + g_ref[...] * ao_ref[...]
        x1_ref[...] = x1
        r = lax.rsqrt(jnp.mean(x1 * x1, axis=-1, keepdims=True) + EPS)
        s = ss_ref[0]
        h_ref[...] = ((x1 * r * nw_ref[...]) * (1.0 + s[1:2]) + s[0:1]).astype(BF16)

    row = pl.BlockSpec((TM, D), lambda i: (i, 0))
    return _pcall(body, name="res1_mod2", grid=(s_len // TM,),
                  in_specs=[row, row, _full((1, D)), _full((1, D)), _full((1, 2, D))],
                  out_specs=[row, row],
                  out_shape=[jax.ShapeDtypeStruct((s_len, D), F32),
                             jax.ShapeDtypeStruct((s_len, D), BF16)])(x, ao, g1, nw2, ss2)


TS = 512


def _acc_call(body, *, name, grid, in_specs, out_specs, out_shape, acc_shapes, args):
    return _pcall(body, name=name, grid=grid, in_specs=in_specs, out_specs=out_specs, out_shape=out_shape,
                  scratch=[pltpu.VMEM(s, F32) for s in acc_shapes])(*args)


def _mm_cs(a, w4, *, name):
    m, k = a.shape
    _, _, ns = w4.shape

    def body(a_ref, w_ref, o_ref):
        o_ref[...] = _bdot(a_ref[...], w_ref[0])

    return _pcall(body, name=name, grid=(m // TS, 4),
                  in_specs=[pl.BlockSpec((TS, k), lambda i, j: (i, 0)),
                            pl.BlockSpec((1, k, ns), lambda i, j: (j, 0, 0))],
                  out_specs=pl.BlockSpec((TS, ns), lambda i, j: (i, j)),
                  out_shape=jax.ShapeDtypeStruct((m, 4 * ns), F32))(a, w4)


def _mm_cs_nt(a, w4, *, name):
    m = a.shape[0]
    _, k, ns = w4.shape

    def body(a_ref, w_ref, o_ref, acc):
        j = pl.program_id(1)

        @pl.when(j == 0)
        def _():
            acc[...] = jnp.zeros_like(acc)

        acc[...] += _bdot(a_ref[...], w_ref[0], NT)

        @pl.when(j == 3)
        def _():
            o_ref[...] = acc[...]

    return _acc_call(body, name=name, grid=(m // TS, 4),
                     in_specs=[pl.BlockSpec((TS, ns), lambda i, j: (i, j)),
                               pl.BlockSpec((1, k, ns), lambda i, j: (j, 0, 0))],
                     out_specs=pl.BlockSpec((TS, k), lambda i, j: (i, 0)),
                     out_shape=jax.ShapeDtypeStruct((m, k), F32), acc_shapes=[(TS, k)], args=(a, w4))


def _mm_cs_tn(a, b, ns, *, name):
    s_len, k = a.shape
    nk = s_len // TS

    def body(a_ref, b_ref, o_ref, acc):
        t = pl.program_id(1)

        @pl.when(t == 0)
        def _():
            acc[...] = jnp.zeros_like(acc)

        acc[...] += _bdot(a_ref[...], b_ref[...], TN)

        @pl.when(t == nk - 1)
        def _():
            o_ref[0] = acc[...]

    return _acc_call(body, name=name, grid=(4, nk),
                     in_specs=[pl.BlockSpec((TS, k), lambda j, t: (t, 0)),
                               pl.BlockSpec((TS, ns), lambda j, t: (t, j))],
                     out_specs=pl.BlockSpec((1, k, ns), lambda j, t: (j, 0, 0)),
                     out_shape=jax.ShapeDtypeStruct((4, k, ns), F32), acc_shapes=[(k, ns)], args=(a, b))


def _ffn_up(h2, g4, u4):
    s_len = h2.shape[0]
    ns = g4.shape[1]

    def body(h_ref, g_ref, u_ref, a_ref, b_ref, z_ref):
        h = h_ref[...]
        a = _bdot(h, g_ref[0], NT)
        b = _bdot(h, u_ref[0], NT)
        a_ref[0] = a
        b_ref[0] = b
        z_ref[0] = (a * _sig(a) * b).astype(BF16)

    w = pl.BlockSpec((1, ns, D), lambda i, j: (j, 0, 0))
    o = pl.BlockSpec((1, TS, ns), lambda i, j: (j, i, 0))
    f = jax.ShapeDtypeStruct((4, s_len, ns), F32)
    return _pcall(body, name="ffn_up", grid=(s_len // TS, 4),
                  in_specs=[pl.BlockSpec((TS, D), lambda i, j: (i, 0)), w, w], out_specs=[o, o, o],
                  out_shape=[f, f, jax.ShapeDtypeStruct((4, s_len, ns), BF16)])(h2, g4, u4)


def _ffn_down(z4, dn4):
    _, s_len, ns = z4.shape

    def body(z_ref, w_ref, o_ref, acc):
        j = pl.program_id(1)

        @pl.when(j == 0)
        def _():
            acc[...] = jnp.zeros_like(acc)

        acc[...] += _bdot(z_ref[0], w_ref[0])

        @pl.when(j == 3)
        def _():
            o_ref[...] = acc[...]

    return _acc_call(body, name="ffn_down", grid=(s_len // TS, 4),
                     in_specs=[pl.BlockSpec((1, TS, ns), lambda i, j: (j, i, 0)),
                               pl.BlockSpec((1, ns, D), lambda i, j: (j, 0, 0))],
                     out_specs=pl.BlockSpec((TS, D), lambda i, j: (i, 0)),
                     out_shape=jax.ShapeDtypeStruct((s_len, D), F32), acc_shapes=[(TS, D)], args=(z4, dn4))


def _ffn_dz(dyb, dn4, a4, b4):
    _, s_len, ns = a4.shape

    def body(dy_ref, w_ref, a_ref, b_ref, da_ref, db_ref):
        dz = _bdot(dy_ref[...], w_ref[0], NT)
        a = a_ref[0]
        s = _sig(a)
        da_ref[0] = (dz * b_ref[0] * (s * (1.0 + a * (1.0 - s)))).astype(BF16)
        db_ref[0] = (dz * (a * s)).astype(BF16)

    t = pl.BlockSpec((1, TS, ns), lambda i, j: (j, i, 0))
    o = jax.ShapeDtypeStruct((4, s_len, ns), BF16)
    return _pcall(body, name="ffn_dz", grid=(s_len // TS, 4),
                  in_specs=[pl.BlockSpec((TS, D), lambda i, j: (i, 0)),
                            pl.BlockSpec((1, ns, D), lambda i, j: (j, 0, 0)), t, t],
                  out_specs=[t, t], out_shape=[o, o])(dyb, dn4, a4, b4)


def _ffn_gdn(z4, dyb):
    _, s_len, ns = z4.shape
    nk = s_len // TS

    def body(z_ref, dy_ref, o_ref, acc):
        t = pl.program_id(1)

        @pl.when(t == 0)
        def _():
            acc[...] = jnp.zeros_like(acc)

        acc[...] += _bdot(z_ref[0], dy_ref[...], TN)

        @pl.when(t == nk - 1)
        def _():
            o_ref[0] = acc[...]

    return _acc_call(body, name="ffn_gdn", grid=(4, nk),
                     in_specs=[pl.BlockSpec((1, TS, ns), lambda j, t: (j, t, 0)),
                               pl.BlockSpec((TS, D), lambda j, t: (t, 0))],
                     out_specs=pl.BlockSpec((1, ns, D), lambda j, t: (j, 0, 0)),
                     out_shape=jax.ShapeDtypeStruct((4, ns, D), F32), acc_shapes=[(ns, D)], args=(z4, dyb))


def _ffn_dh2(da4, db4, g4, u4):
    _, s_len, ns = da4.shape

    def body(da_ref, db_ref, g_ref, u_ref, o_ref, acc):
        j = pl.program_id(1)

        @pl.when(j == 0)
        def _():
            acc[...] = jnp.zeros_like(acc)

        acc[...] += _bdot(da_ref[0], g_ref[0]) + _bdot(db_ref[0], u_ref[0])

        @pl.when(j == 3)
        def _():
            o_ref[...] = acc[...]

    t = pl.BlockSpec((1, TS, ns), lambda i, j: (j, i, 0))
    w = pl.BlockSpec((1, ns, D), lambda i, j: (j, 0, 0))
    return _acc_call(body, name="ffn_dh2", grid=(s_len // TS, 4), in_specs=[t, t, w, w],
                     out_specs=pl.BlockSpec((TS, D), lambda i, j: (i, 0)),
                     out_shape=jax.ShapeDtypeStruct((s_len, D), F32), acc_shapes=[(TS, D)],
                     args=(da4, db4, g4, u4))


def _ffn_ggu(h2, da4, db4):
    _, s_len, ns = da4.shape
    nk = s_len // TS

    def body(h_ref, da_ref, db_ref, gg_ref, gu_ref, acc_g, acc_u):
        t = pl.program_id(1)

        @pl.when(t == 0)
        def _():
            acc_g[...] = jnp.zeros_like(acc_g)
            acc_u[...] = jnp.zeros_like(acc_u)

        h = h_ref[...]
        acc_g[...] += _bdot(da_ref[0], h, TN)
        acc_u[...] += _bdot(db_ref[0], h, TN)

        @pl.when(t == nk - 1)
        def _():
            gg_ref[0] = acc_g[...]
            gu_ref[0] = acc_u[...]

    d = pl.BlockSpec((1, TS, ns), lambda j, t: (j, t, 0))
    o = pl.BlockSpec((1, ns, D), lambda j, t: (j, 0, 0))
    f = jax.ShapeDtypeStruct((4, ns, D), F32)
    return _acc_call(body, name="ffn_ggu", grid=(4, nk),
                     in_specs=[pl.BlockSpec((TS, D), lambda j, t: (t, 0)), d, d], out_specs=[o, o],
                     out_shape=[f, f], acc_shapes=[(ns, D), (ns, D)], args=(h2, da4, db4))


def _loss_head(x1, y, g2, tgt):
    s_len = x1.shape[0]

    def body(x1_ref, y_ref, g_ref, t_ref, sq_ref, dx2_ref, dyb_ref, dg_ref):
        i = pl.program_id(0)

        @pl.when(i == 0)
        def _():
            sq_ref[...] = jnp.zeros_like(sq_ref)
            dg_ref[...] = jnp.zeros_like(dg_ref)

        y_ = y_ref[...]
        g = g_ref[...]
        e = x1_ref[...] + g * y_ - t_ref[...]
        sq_ref[...] += jnp.sum(e * e, axis=0, keepdims=True)
        dx2 = e * (1.0 / D)
        dx2_ref[...] = dx2
        dyb_ref[...] = (g * dx2).astype(BF16)
        dg_ref[...] += jnp.sum(dx2 * y_, axis=0, keepdims=True)

    row = pl.BlockSpec((TM, D), lambda i: (i, 0))
    vec = _full((1, D))
    return _pcall(body, name="loss_head", grid=(s_len // TM,),
                  in_specs=[row, row, vec, row], out_specs=[vec, row, row, vec],
                  out_shape=[jax.ShapeDtypeStruct((1, D), F32), jax.ShapeDtypeStruct((s_len, D), F32),
                             jax.ShapeDtypeStruct((s_len, D), BF16),
                             jax.ShapeDtypeStruct((1, D), F32)])(x1, y, g2, tgt)


def _mod2_bwd(x1, dh2, dx2, ao, nw2, ss2, g1):
    s_len = x1.shape[0]

    def body(x1_ref, dh_ref, dx2_ref, ao_ref, nw_ref, ss_ref, g_ref,
             dx1_ref, da_ref, dss_ref, dnw_ref, dg_ref):
        i = pl.program_id(0)

        @pl.when(i == 0)
        def _():
            dss_ref[...] = jnp.zeros_like(dss_ref)
            dnw_ref[...] = jnp.zeros_like(dnw_ref)
            dg_ref[...] = jnp.zeros_like(dg_ref)

        dh = dh_ref[...]
        nw = nw_ref[...]
        scale = ss_ref[0][1:2]
        dxn, xh = _norm_bwd_rows(x1_ref[...], dh, nw, scale)
        dx1 = dx2_ref[...] + dxn
        dx1_ref[...] = dx1
        da_ref[...] = (g_ref[...] * dx1).astype(BF16)
        dg_ref[...] += jnp.sum(dx1 * ao_ref[...], axis=0, keepdims=True)
        dsh = jnp.sum(dh, axis=0, keepdims=True)
        dsc = jnp.sum(dh * xh * nw, axis=0, keepdims=True)
        dss_ref[...] += jnp.concatenate([dsh, dsc], axis=0)
        dnw_ref[...] += jnp.sum(dh * xh * (1.0 + scale), axis=0, keepdims=True)

    row = pl.BlockSpec((TM, D), lambda i: (i, 0))
    vec = _full((1, D))
    return _pcall(body, name="mod2_bwd", grid=(s_len // TM,),
                  in_specs=[row, row, row, row, vec, _full((1, 2, D)), vec],
                  out_specs=[row, row, _full((2, D)), vec, vec],
                  out_shape=[jax.ShapeDtypeStruct((s_len, D), F32), jax.ShapeDtypeStruct((s_len, D), BF16),
                             jax.ShapeDtypeStruct((2, D), F32), jax.ShapeDtypeStruct((1, D), F32),
                             jax.ShapeDtypeStruct((1, D), F32)])(x1, dh2, dx2, ao, nw2, ss2, g1)


def _mod1_bwd(tok, dh, dx1, nw1, ss1):
    tt = tok.shape[0]
    s_len = dx1.shape[0]

    def body(t_ref, dh_ref, dx1_ref, nw_ref, ss_ref, dx_ref, dss_ref, dnw_ref):
        i = pl.program_id(0)

        @pl.when(i == 0)
        def _():
            dnw_ref[...] = jnp.zeros_like(dnw_ref)

        @pl.when(i <= 1)
        def _():
            dss_ref[...] = jnp.zeros_like(dss_ref)

        dh_ = dh_ref[...]
        nw = nw_ref[...]
        scale = ss_ref[0][1:2]
        dxn, xh = _norm_bwd_rows(t_ref[...], dh_, nw, scale)

        @pl.when(i >= 1)
        def _():
            dx_ref[...] = dx1_ref[...] + dxn

        dsh = jnp.sum(dh_, axis=0, keepdims=True)
        dsc = jnp.sum(dh_ * xh * nw, axis=0, keepdims=True)
        dss_ref[...] += jnp.concatenate([dsh, dsc], axis=0)[None]
        dnw_ref[...] += jnp.sum(dh_ * xh * (1.0 + scale), axis=0, keepdims=True)

    row = pl.BlockSpec((TM, D), lambda i: (i, 0))
    lat = pl.BlockSpec((TM, D), lambda i: (jnp.maximum(i - 1, 0), 0))
    sel = pl.BlockSpec((1, 2, D), lambda i: (jnp.minimum(i, 1), 0, 0))
    return _pcall(body, name="mod1_bwd", grid=(tt // TM,),
                  in_specs=[row, row, lat, _full((1, D)), sel],
                  out_specs=[lat, sel, _full((1, D))],
                  out_shape=[jax.ShapeDtypeStruct((s_len, D), F32), jax.ShapeDtypeStruct((2, 2, D), F32),
                             jax.ShapeDtypeStruct((1, D), F32)])(tok, dh, dx1, nw1, ss1)


def _tri(rev, transpose=False):
    r = lax.broadcasted_iota(jnp.int32, (CH, CH), 0)
    c = lax.broadcasted_iota(jnp.int32, (CH, CH), 1)
    lower = (c >= r) if (rev != transpose) else (c <= r)
    return lower


def _hgrn_gate(fl, qraw, lg):
    lb = 1.0 / (1.0 + jnp.exp(lg[1:2] - lg[0:1]))
    sg = _sig(fl)
    f = lb + (1.0 - lb) * sg
    q = qraw * _sig(qraw) * (HGD ** -0.5)
    return lb, sg, f, q


def _hgrn_fwd(p, lg, *, rev):
    tt = p.shape[0]
    nt = tt // TM
    ncht = TM // CH
    d = 1 if rev else 0

    def tile_of(s):
        return jnp.where(s == 0, 0, nt - s) if rev else s

    def body(f_ref, inp_ref, q_ref, lg_ref, o_ref, st_ref, lf_s, k_s, q_s, state):
        s = pl.program_id(0)

        @pl.when(s == 0)
        def _():
            state[...] = jnp.zeros_like(state)

        _, _, f, q = _hgrn_gate(f_ref[...], q_ref[...], lg_ref[0])
        lf_s[...] = jnp.log(f)
        k_s[...] = 1.0 - f
        q_s[...] = q
        tri = _tri(rev)
        trif = tri.astype(F32)

        def chunk(cc, carry):
            c = (ncht - 1 - cc) if rev else cc
            r0 = pl.multiple_of(c * CH, CH)
            lf = lf_s[pl.ds(r0, CH), :]
            cum = _dot(trif, lf, prec=HI)
            tot = jnp.sum(lf, axis=0, keepdims=True)
            qd = q_s[pl.ds(r0, CH), :] * jnp.exp(cum)
            kk = k_s[pl.ds(r0, CH), :]
            kd = kk * jnp.exp(-cum)
            ke = kk * jnp.exp(tot - cum)
            et = jnp.exp(tot)
            v = inp_ref[pl.ds(r0, CH), :]
            outs = []
            for h in range(4):
                sl = slice(h * HGD, (h + 1) * HGD)
                st0 = state[h]
                st_ref[c, h] = st0
                pm = jnp.where(tri, _bdot(qd[:, sl], kd[:, sl], NT), 0.0)
                outs.append(_bdot(pm, v[:, sl]) + _bdot(qd[:, sl], st0, NT))
                state[h] = st0 * et[:, sl] + _bdot(v[:, sl], ke[:, sl], TN)
            o_ref[pl.ds(r0, CH), :] = jnp.concatenate(outs, axis=1)
            return carry

        lax.fori_loop(0, ncht, chunk, 0)

    def col(cb):
        return pl.BlockSpec((TM, HGW), lambda s: (tile_of(s), cb))

    return _pcall(
        body, name="hgrn_fwd_rev" if rev else "hgrn_fwd", grid=(nt,),
        in_specs=[col(C_FB if rev else C_FF), col(C_INP), col(C_QHG),
                  pl.BlockSpec((1, 2, HGW), lambda s: (d, 0, 0))],
        out_specs=[pl.BlockSpec((TM, HGW), lambda s: (tile_of(s), 0)),
                   pl.BlockSpec((ncht, 4, HGD, HGD), lambda s: (tile_of(s), 0, 0, 0))],
        out_shape=[jax.ShapeDtypeStruct((tt, HGW), F32),
                   jax.ShapeDtypeStruct((nt * ncht, 4, HGD, HGD), F32)],
        scratch=[pltpu.VMEM((TM, HGW), F32)] * 3 + [pltpu.VMEM((4, HGD, HGD), F32)])(p, p, p, lg)


def _hgrn_bwd(p, lg, do, st, dp, prev, *, rev):
    tt = p.shape[0]
    nt = tt // TM
    ncht = TM // CH
    d = 1 if rev else 0
    second = prev is not None

    def tile_of(s):
        return jnp.where(s == nt - 1, 0, s + 1) if rev else nt - 1 - s

    def body(*refs):
        if second:
            (f_ref, inp_ref, q_ref, lg_ref, do_ref, st_ref, dvp_ref, dqp_ref, _dp_in,
             dp_ref, dlg_ref, lf_s, k_s, q_s, do_s, dq_s, dk_s, dv_s, dlf_s, dstate) = refs
        else:
            (f_ref, inp_ref, q_ref, lg_ref, do_ref, st_ref, _dp_in,
             dp_ref, dv_ref, dq_ref, dlg_ref, lf_s, k_s, q_s, do_s, dq_s, dk_s, dv_s, dlf_s, dstate) = refs
        s = pl.program_id(0)
        tile = tile_of(s)

        @pl.when(s == 0)
        def _():
            dstate[...] = jnp.zeros_like(dstate)
            dlg_ref[...] = jnp.zeros_like(dlg_ref)

        qraw = q_ref[...]
        lb, sg, f, q = _hgrn_gate(f_ref[...], qraw, lg_ref[0])
        lf_s[...] = jnp.log(f)
        k_s[...] = 1.0 - f
        q_s[...] = q
        do_s[...] = jnp.where(tile == 0, 0.0, do_ref[...])
        tri = _tri(rev)
        trit = _tri(rev, transpose=True)
        tritf = trit.astype(F32)
        trif = tri.astype(F32)

        def chunk(cc, carry):
            c = cc if rev else (ncht - 1 - cc)
            r0 = pl.multiple_of(c * CH, CH)
            lf = lf_s[pl.ds(r0, CH), :]
            cum = _dot(trif, lf, prec=HI)
            tot = jnp.sum(lf, axis=0, keepdims=True)
            ea = jnp.exp(cum)
            eb = jnp.exp(-cum)
            ee = jnp.exp(tot - cum)
            et = jnp.exp(tot)
            qd = q_s[pl.ds(r0, CH), :] * ea
            kk = k_s[pl.ds(r0, CH), :]
            kd = kk * eb
            ke = kk * ee
            v = inp_ref[pl.ds(r0, CH), :]
            doc = do_s[pl.ds(r0, CH), :]
            dq_l, dk_l, dv_l, dcum_l, dtot_l = [], [], [], [], []
            for h in range(4):
                sl = slice(h * HGD, (h + 1) * HGD)
                qd_, kd_, ke_, v_, do_ = qd[:, sl], kd[:, sl], ke[:, sl], v[:, sl], doc[:, sl]
                st0 = st_ref[c, h]
                ds1 = dstate[h]
                pmt = jnp.where(trit, _bdot(kd_, qd_, NT), 0.0)
                dpm = jnp.where(tri, _bdot(do_, v_, NT), 0.0)
                dpmt = jnp.where(trit, _bdot(v_, do_, NT), 0.0)
                dv = _bdot(pmt, do_) + _bdot(ke_, ds1, NT)
                dqd = _bdot(dpm, kd_) + _bdot(do_, st0)
                dkd = _bdot(dpmt, qd_)
                dke = _bdot(v_, ds1)
                dstate[h] = ds1 * et[:, sl] + _bdot(do_, qd_, TN)
                dtot_l.append(jnp.sum(ds1 * st0, axis=0, keepdims=True) * et[:, sl]
                              + jnp.sum(dke * ke_, axis=0, keepdims=True))
                dq_l.append(dqd * ea[:, sl])
                dk_l.append(dkd * eb[:, sl] + dke * ee[:, sl])
                dv_l.append(dv)
                dcum_l.append(dqd * qd_ - dkd * kd_ - dke * ke_)
            dcum = jnp.concatenate(dcum_l, axis=1)
            dlf = _dot(tritf, dcum, prec=HI) + jnp.concatenate(dtot_l, axis=1)
            dq_s[pl.ds(r0, CH), :] = jnp.concatenate(dq_l, axis=1)
            dk_s[pl.ds(r0, CH), :] = jnp.concatenate(dk_l, axis=1)
            dv_s[pl.ds(r0, CH), :] = jnp.concatenate(dv_l, axis=1)
            dlf_s[pl.ds(r0, CH), :] = dlf
            return carry

        lax.fori_loop(0, ncht, chunk, 0)

        df = dlf_s[...] / f - dk_s[...]
        dfl = df * (1.0 - lb) * sg * (1.0 - sg)
        dlb = jnp.sum(df * (1.0 - sg), axis=0, keepdims=True)
        dl0 = dlb * lb * (1.0 - lb)
        dlg_ref[...] += jnp.concatenate([dl0, -dl0], axis=0)[None]
        if second:
            sq = _sig(qraw)
            dqr = (dqp_ref[...] + dq_s[...]) * (HGD ** -0.5) * (sq * (1.0 + qraw * (1.0 - sq)))
            dp_ref[...] = jnp.concatenate([dfl, dvp_ref[...] + dv_s[...], dqr], axis=1).astype(BF16)
        else:
            dp_ref[...] = dfl.astype(BF16)
            dv_ref[...] = dv_s[...]
            dq_ref[...] = dq_s[...]

    def col(cb):
        return pl.BlockSpec((TM, HGW), lambda s: (tile_of(s), cb))

    tok = pl.BlockSpec((TM, HGW), lambda s: (tile_of(s), 0))
    in_specs = [col(C_FB if rev else C_FF), col(C_INP), col(C_QHG),
                pl.BlockSpec((1, 2, HGW), lambda s: (d, 0, 0)),
                pl.BlockSpec((TM, HGW), lambda s: (jnp.maximum(tile_of(s) - 1, 0), 0)),
                pl.BlockSpec((ncht, 4, HGD, HGD), lambda s: (tile_of(s), 0, 0, 0))]
    args = [p, p, p, lg, do, st]
    dlg_spec = _full((1, 2, HGW))
    dlg_shape = jax.ShapeDtypeStruct((1, 2, HGW), F32)
    if second:
        in_specs += [tok, tok]
        args += [prev[0], prev[1]]
        out_specs = [pl.BlockSpec((TM, 3 * HGW), lambda s: (tile_of(s), 0)), dlg_spec]
        out_shape = [jax.ShapeDtypeStruct(dp.shape, BF16), dlg_shape]
    else:
        out_specs = [pl.BlockSpec((TM, HGW), lambda s: (tile_of(s), C_FB if rev else C_FF)), tok, tok, dlg_spec]
        out_shape = [jax.ShapeDtypeStruct(dp.shape, BF16), jax.ShapeDtypeStruct((tt, HGW), F32),
                     jax.ShapeDtypeStruct((tt, HGW), F32), dlg_shape]
    in_specs.append(ANY)
    args.append(dp)
    return _pcall(body, name="hgrn_bwd_rev" if rev else "hgrn_bwd", grid=(nt,),
                  in_specs=in_specs, out_specs=out_specs, out_shape=out_shape,
                  scratch=[pltpu.VMEM((TM, HGW), F32)] * 8 + [pltpu.VMEM((4, HGD, HGD), F32)],
                  aliases={len(args) - 1: 0})(*args)


def _head_rms(o, w, nheads):
    outs = []
    for h in range(nheads):
        oh = o[:, h * HGD:(h + 1) * HGD]
        outs.append(oh * lax.rsqrt(jnp.mean(oh * oh, axis=-1, keepdims=True) + EPS))
    return jnp.concatenate(outs, axis=1)


def _readout(o0, o1, p, hw4):
    s_len = o0.shape[0] - L

    def body(o0_ref, o1_ref, g_ref, w_ref, y_ref):
        xh = _head_rms(o0_ref[...] + o1_ref[...], None, 4)
        g = g_ref[...]
        y_ref[...] = (xh * w_ref[...] * (g * _sig(g))).astype(BF16)

    lat = pl.BlockSpec((TM, HGW), lambda i: (i + 1, 0))
    return _pcall(body, name="readout", grid=(s_len // TM,),
                  in_specs=[lat, lat, pl.BlockSpec((TM, HGW), lambda i: (i + 1, C_GHG)), _full((1, HGW))],
                  out_specs=pl.BlockSpec((TM, HGW), lambda i: (i, 0)),
                  out_shape=jax.ShapeDtypeStruct((s_len, HGW), BF16))(o0, o1, p, hw4)


def _readout_bwd(o0, o1, p, hw4, dy, dp):
    tt = o0.shape[0]
    s_len = tt - L

    def body(o0_ref, o1_ref, g_ref, w_ref, dy_ref, _dp_in, dp_ref, do_ref, dw_ref):
        i = pl.program_id(0)

        @pl.when(i == 0)
        def _():
            dw_ref[...] = jnp.zeros_like(dw_ref)
            dp_ref[...] = jnp.zeros_like(dp_ref)

        @pl.when(i >= 1)
        def _():
            o = o0_ref[...] + o1_ref[...]
            g = g_ref[...]
            w = w_ref[...]
            sg = _sig(g)
            dy_ = dy_ref[...]
            dsw = dy_ * (g * sg)
            outs, xhs = [], []
            for h in range(4):
                sl = slice(h * HGD, (h + 1) * HGD)
                oh = o[:, sl]
                r = lax.rsqrt(jnp.mean(oh * oh, axis=-1, keepdims=True) + EPS)
                xh = oh * r
                dxh = dsw[:, sl] * w[:, sl]
                outs.append(r * (dxh - xh * jnp.mean(dxh * xh, axis=-1, keepdims=True)))
                xhs.append(xh)
            xh = jnp.concatenate(xhs, axis=1)
            do_ref[...] = jnp.concatenate(outs, axis=1)
            dp_ref[...] = (dy_ * xh * w * (sg * (1.0 + g * (1.0 - sg)))).astype(BF16)
            dw_ref[...] += jnp.sum(dsw * xh, axis=0, keepdims=True)

    tok = pl.BlockSpec((TM, HGW), lambda i: (i, 0))
    lat = pl.BlockSpec((TM, HGW), lambda i: (jnp.maximum(i - 1, 0), 0))
    return _pcall(body, name="readout_bwd", grid=(tt // TM,),
                  in_specs=[tok, tok, pl.BlockSpec((TM, HGW), lambda i: (i, C_GHG)), _full((1, HGW)), lat, ANY],
                  out_specs=[pl.BlockSpec((TM, HGW), lambda i: (i, C_GHG)), lat, _full((1, HGW))],
                  out_shape=[jax.ShapeDtypeStruct(dp.shape, BF16), jax.ShapeDtypeStruct((s_len, HGW), F32),
                             jax.ShapeDtypeStruct((1, HGW), F32)],
                  aliases={5: 0})(o0, o1, p, hw4, dy, dp)


def _rope_tables(s_len):
    t = np.arange(s_len)
    inv = ROPE_THETA ** (-np.arange(0, 32, 2, dtype=np.float64) / 32)
    def half(pos):
        ang = pos[:, None].astype(np.float64) * inv[None, :]
        return (np.concatenate([np.cos(ang), np.cos(ang)], 1), np.concatenate([-np.sin(ang), np.sin(ang)], 1))
    cr, sr = half(t // GRID_W)
    cc, sc = half(t % GRID_W)
    cos = np.concatenate([cr, cc, cr, cc], 1)
    sin = np.concatenate([sr, sc, sr, sc], 1)
    cos = np.concatenate([np.ones((L, 128)), cos], 0)
    sin = np.concatenate([np.zeros((L, 128)), sin], 0)
    return jnp.asarray(cos, F32), jnp.asarray(sin, F32)


def _blockdiag(n, w):
    i = np.arange(n)
    return jnp.asarray((i[:, None] // w == i[None, :] // w) / float(w), F32)


def _dup_matrix():
    m = np.zeros((128, 512), np.float32)
    for g in range(2):
        for j in range(4):
            for dd in range(HDIM):
                m[64 * g + dd, 256 * g + 64 * j + dd] = 1.0
    return m


def _rot(x):
    n = x.shape[1]
    lane = lax.broadcasted_iota(jnp.int32, x.shape, 1)
    return jnp.where((lane % 32) < 16, pltpu.roll(x, n - 16, 1), pltpu.roll(x, 16, 1))


def _qk_prep(p, cos, sin, qnw8, knw2, bd512, bd128, dup):
    tt = p.shape[0]

    def body(q_ref, kv_ref, cos_ref, sin_ref, qw_ref, kw_ref, b5_ref, b1_ref, dup_ref,
             qr_ref, k4_ref, v4_ref):
        cos_, sin_ = cos_ref[...], sin_ref[...]
        q = q_ref[...]
        qn = q * lax.rsqrt(_dot(q * q, b5_ref[...], prec=HI) + EPS) * qw_ref[...]
        cos4 = jnp.concatenate([cos_] * 4, axis=1)
        sin4 = jnp.concatenate([sin_] * 4, axis=1)
        qr_ref[...] = ((qn * cos4 + _rot(qn) * sin4) * (HDIM ** -0.5)).astype(BF16)
        kv = kv_ref[...]
        k, v = kv[:, :128], kv[:, 128:]
        kn = k * lax.rsqrt(_dot(k * k, b1_ref[...], prec=HI) + EPS) * kw_ref[...]
        kr = kn * cos_ + _rot(kn) * sin_
        k4_ref[...] = _bdot(kr, dup_ref[...]).astype(BF16)
        v4_ref[...] = _bdot(v, dup_ref[...]).astype(BF16)

    row = lambda w, cb: pl.BlockSpec((TM, w), lambda i: (i, cb))
    out = jax.ShapeDtypeStruct((tt, ATW), BF16)
    return _pcall(body, name="qk_prep", grid=(tt // TM,),
                  in_specs=[row(ATW, C_QRAW), row(256, C_KV), row(128, 0), row(128, 0),
                            _full((1, ATW)), _full((1, 128)), _full((ATW, ATW)), _full((128, 128)),
                            _full((128, ATW))],
                  out_specs=[row(ATW, 0)] * 3, out_shape=[out] * 3)(
                      p, p, cos, sin, qnw8, knw2, bd512, bd128, dup)


def _attn_masks(i, nb):
    r = lax.broadcasted_iota(jnp.int32, (BLK, 3 * BLK + L), 0)
    c = lax.broadcasted_iota(jnp.int32, (BLK, 3 * BLK + L), 1)
    kpos = (i - 1) * BLK + c
    loc = (jnp.abs(c - BLK - r) <= BLK) & (kpos >= 0) & (kpos < nb * BLK)
    return loc | (c >= 3 * BLK)


def _lane_mask(j):
    lane = lax.broadcasted_iota(jnp.int32, (1, 256), 1)
    return (lane // HDIM) == j


def _attn_specs(nb):
    blk = lambda off: pl.BlockSpec((BLK, ATW), lambda i: (jnp.clip(i + off, 0, nb - 1) + 2, 0))
    ctx = pl.BlockSpec((L, ATW), lambda i: (0, 0))
    return blk, ctx


def _attn_fwd(qr, k4, v4, sinks):
    tt = qr.shape[0]
    s_len = tt - L
    nb = s_len // BLK

    def body(sk_ref, q_ref, kp, ko, kn, kc, vp, vo, vn, vc, y_ref, lse_ref):
        i = pl.program_id(0)
        valid = _attn_masks(i, nb)
        q = q_ref[...]
        ys, lses = [], []
        for g in range(2):
            gs = slice(256 * g, 256 * g + 256)
            kcat = jnp.concatenate([kp[:, gs], ko[:, gs], kn[:, gs], kc[:, gs]], axis=0)
            vcat = jnp.concatenate([vp[:, gs], vo[:, gs], vn[:, gs], vc[:, gs]], axis=0)
            qg = q[:, gs]
            og = jnp.zeros((BLK, 256), F32)
            lg = jnp.zeros((BLK, 256), F32)
            for j in range(4):
                lm = _lane_mask(j)
                sink = sk_ref[4 * g + j]
                s = jnp.where(valid, _dot(jnp.where(lm, qg, jnp.zeros_like(qg)), kcat, NT), -1e30)
                m = jnp.maximum(jnp.max(s, axis=-1, keepdims=True), sink)
                e = jnp.exp(s - m)
                den = jnp.sum(e, axis=-1, keepdims=True) + jnp.exp(sink - m)
                pr = e / den
                og = og + jnp.where(lm, _bdot(pr, vcat), 0.0)
                lg = lg + jnp.where(lm, m + jnp.log(den), 0.0)
            ys.append(og)
            lses.append(lg)
        y_ref[...] = jnp.concatenate(ys, axis=1).astype(BF16)
        lse_ref[...] = jnp.concatenate(lses, axis=1)

    blk, ctx = _attn_specs(nb)
    out = pl.BlockSpec((BLK, ATW), lambda i: (i, 0))
    return _pcall(body, name="attn_fwd", grid=(nb,),
                  in_specs=[pl.BlockSpec(memory_space=pltpu.SMEM), blk(0),
                            blk(-1), blk(0), blk(1), ctx, blk(-1), blk(0), blk(1), ctx],
                  out_specs=[out, out],
                  out_shape=[jax.ShapeDtypeStruct((s_len, ATW), BF16),
                             jax.ShapeDtypeStruct((s_len, ATW), F32)])(
                      sinks, qr, k4, k4, k4, k4, v4, v4, v4, v4)


def _attn_bwd(qr, k4, v4, sinks, y, lse, dy):
    tt = qr.shape[0]
    s_len = tt - L
    nb = s_len // BLK

    def body(sk_ref, q_ref, kp, ko, kn, kc, vp, vo, vn, vc, y_ref, lse_ref, dy_ref,
             dq_ref, dkw_ref, dvw_ref, dkc_ref, dvc_ref, dsk_ref):
        i = pl.program_id(0)

        @pl.when(i == 0)
        def _():
            dkc_ref[...] = jnp.zeros_like(dkc_ref)
            dvc_ref[...] = jnp.zeros_like(dvc_ref)
            dsk_ref[...] = jnp.zeros_like(dsk_ref)

        valid = _attn_masks(i, nb)
        q = q_ref[...]
        dy_ = dy_ref[...]
        dly = dy_ * y_ref[...].astype(F32)
        lse_ = lse_ref[...]
        dqs = []
        for g in range(2):
            gs = slice(256 * g, 256 * g + 256)
            kcat = jnp.concatenate([kp[:, gs], ko[:, gs], kn[:, gs], kc[:, gs]], axis=0)
            vcat = jnp.concatenate([vp[:, gs], vo[:, gs], vn[:, gs], vc[:, gs]], axis=0)
            qg, dyg, dlg, lsg = q[:, gs], dy_[:, gs], dly[:, gs], lse_[:, gs]
            dqg = jnp.zeros((BLK, 256), F32)
            dkg = jnp.zeros((3 * BLK + L, 256), F32)
            dvg = jnp.zeros((3 * BLK + L, 256), F32)
            for j in range(4):
                lm = _lane_mask(j)
                sink = sk_ref[4 * g + j]
                qm = jnp.where(lm, qg, jnp.zeros_like(qg))
                dym = jnp.where(lm, dyg, 0.0).astype(BF16)
                lse_h = jnp.max(jnp.where(lm, lsg, -1e30), axis=-1, keepdims=True)
                delta = jnp.sum(jnp.where(lm, dlg, 0.0), axis=-1, keepdims=True)
                s = _dot(qm, kcat, NT)
                pr = jnp.where(valid, jnp.exp(s - lse_h), 0.0)
                dpr = _dot(dym, vcat, NT)
                dsc = pr * (dpr - delta)
                psink = jnp.exp(sink - lse_h)
                dsk_ref[4 * g + j:4 * g + j + 1, :] += jnp.broadcast_to(
                    -jnp.sum(psink * delta, axis=0, keepdims=True), (1, 128))
                dsb = dsc.astype(BF16)
                dqg = dqg + jnp.where(lm, _dot(dsb, kcat), 0.0)
                dkg = dkg + _dot(dsb, qm, TN)
                dvg = dvg + _dot(pr.astype(BF16), dym, TN)
            dqs.append(dqg)
            dkw_ref[0, :, gs] = dkg[:3 * BLK]
            dvw_ref[0, :, gs] = dvg[:3 * BLK]
            dkc_ref[:, gs] += dkg[3 * BLK:]
            dvc_ref[:, gs] += dvg[3 * BLK:]
        dq_ref[...] = jnp.concatenate(dqs, axis=1)

    blk, ctx = _attn_specs(nb)
    out = pl.BlockSpec((BLK, ATW), lambda i: (i, 0))
    win = pl.BlockSpec((1, 3 * BLK, ATW), lambda i: (i, 0, 0))
    acc = _full((L, ATW))
    return _pcall(body, name="attn_bwd", grid=(nb,),
                  in_specs=[pl.BlockSpec(memory_space=pltpu.SMEM), blk(0),
                            blk(-1), blk(0), blk(1), ctx, blk(-1), blk(0), blk(1), ctx, out, out, out],
                  out_specs=[out, win, win, acc, acc, _full((8, 128))],
                  out_shape=[jax.ShapeDtypeStruct((s_len, ATW), F32),
                             jax.ShapeDtypeStruct((nb, 3 * BLK, ATW), F32),
                             jax.ShapeDtypeStruct((nb, 3 * BLK, ATW), F32),
                             jax.ShapeDtypeStruct((L, ATW), F32), jax.ShapeDtypeStruct((L, ATW), F32),
                             jax.ShapeDtypeStruct((8, 128), F32)])(
                      sinks, qr, k4, k4, k4, k4, v4, v4, v4, v4, y, lse, dy)


def _attn_post(p, cos, sin, qnw8, knw2, bd512, bd128, dupt, dq, dkw, dvw, dkc, dvc, dp):
    tt = p.shape[0]
    s_len = tt - L
    nb = s_len // BLK
    nctx = L // BLK

    def body(q_ref, kv_ref, cos_ref, sin_ref, qw_ref, kw_ref, b5_ref, b1_ref, dupt_ref,
             dq_ref, kwp, kwo, kwn, vwp, vwo, vwn, dkc_ref, dvc_ref, _dp_in,
             dp_ref, dqw_ref, dkw_ref):
        t = pl.program_id(0)
        j = t - nctx

        @pl.when(t == 0)
        def _():
            dqw_ref[...] = jnp.zeros_like(dqw_ref)
            dkw_ref[...] = jnp.zeros_like(dkw_ref)

        is_lat = t >= nctx
        cos_, sin_ = cos_ref[...], sin_ref[...]
        has_p = is_lat & (j >= 1)
        has_n = is_lat & (j <= nb - 2)
        dk4 = (jnp.where(is_lat, kwo[0], dkc_ref[...]) + jnp.where(has_p, kwp[0], 0.0)
               + jnp.where(has_n, kwn[0], 0.0))
        dv4 = (jnp.where(is_lat, vwo[0], dvc_ref[...]) + jnp.where(has_p, vwp[0], 0.0)
               + jnp.where(has_n, vwn[0], 0.0))
        dkr = _dot(dk4, dupt_ref[...], prec=HI)
        dv = _dot(dv4, dupt_ref[...], prec=HI)
        kv = kv_ref[...]
        k = kv[:, :128]
        kw = kw_ref[...]
        rk = lax.rsqrt(_dot(k * k, b1_ref[...], prec=HI) + EPS)
        xk = k * rk
        dkn = dkr * cos_ + _rot(dkr * sin_)
        dxk = dkn * kw
        dk = rk * (dxk - xk * _dot(dxk * xk, b1_ref[...], prec=HI))
        dkw_ref[...] += jnp.sum(dkn * xk, axis=0, keepdims=True)
        q = q_ref[...]
        qw = qw_ref[...]
        rq = lax.rsqrt(_dot(q * q, b5_ref[...], prec=HI) + EPS)
        xq = q * rq
        cos4 = jnp.concatenate([cos_] * 4, axis=1)
        sin4 = jnp.concatenate([sin_] * 4, axis=1)
        dqr = jnp.where(is_lat, dq_ref[...], 0.0) * (HDIM ** -0.5)
        dqn = dqr * cos4 + _rot(dqr * sin4)
        dxq = dqn * qw
        dqraw = rq * (dxq - xq * _dot(dxq * xq, b5_ref[...], prec=HI))
        dqw_ref[...] += jnp.sum(dqn * xq, axis=0, keepdims=True)
        dp_ref[...] = jnp.concatenate([dqraw, dk, dv], axis=1).astype(BF16)

    row = lambda w, cb: pl.BlockSpec((BLK, w), lambda t: (t, cb))
    lat = pl.BlockSpec((BLK, ATW), lambda t: (jnp.maximum(t - nctx, 0), 0))

    def part(off):
        return pl.BlockSpec((1, BLK, ATW), lambda t: (jnp.clip(t - nctx + off, 0, nb - 1), 1 - off, 0))

    cacc = pl.BlockSpec((BLK, ATW), lambda t: (jnp.minimum(t, nctx - 1), 0))
    return _pcall(body, name="attn_post", grid=(tt // BLK,),
                  in_specs=[row(ATW, C_QRAW), row(256, C_KV), row(128, 0), row(128, 0),
                            _full((1, ATW)), _full((1, 128)), _full((ATW, ATW)), _full((128, 128)),
                            _full((ATW, 128)), lat, part(-1), part(0), part(1), part(-1), part(0), part(1),
                            cacc, cacc, ANY],
                  out_specs=[pl.BlockSpec((BLK, 768), lambda t: (t, C_QKV)), _full((1, ATW)), _full((1, 128))],
                  out_shape=[jax.ShapeDtypeStruct(dp.shape, BF16), jax.ShapeDtypeStruct((1, ATW), F32),
                             jax.ShapeDtypeStruct((1, 128), F32)],
                  aliases={18: 0})(p, p, cos, sin, qnw8, knw2, bd512, bd128, dupt,
                                   dq, dkw, dkw, dkw, dvw, dvw, dvw, dkc, dvc, dp)


def _merge(ah, aa, p):
    s_len = ah.shape[0]

    def body(ah_ref, aa_ref, gh_ref, ga_ref, m_ref):
        m_ref[...] = (_sig(gh_ref[...]) * ah_ref[...] + _sig(ga_ref[...]) * aa_ref[...]).astype(BF16)

    row = pl.BlockSpec((TM, D), lambda i: (i, 0))
    return _pcall(body, name="merge", grid=(s_len // TM,),
                  in_specs=[row, row, pl.BlockSpec((TM, D), lambda i: (i + 1, 2)),
                            pl.BlockSpec((TM, D), lambda i: (i + 1, 3))],
                  out_specs=row, out_shape=jax.ShapeDtypeStruct((s_len, D), BF16))(ah, aa, p, p)


def _merge_bwd(dm, ah, aa, p):
    tt = p.shape[0]
    s_len = tt - L

    def body(dm_ref, ah_ref, aa_ref, gh_ref, ga_ref, dp_ref, dmh_ref, dma_ref):
        i = pl.program_id(0)

        @pl.when(i == 0)
        def _():
            dp_ref[...] = jnp.zeros_like(dp_ref)

        @pl.when(i >= 1)
        def _():
            dm_ = dm_ref[...]
            sh, sa = _sig(gh_ref[...]), _sig(ga_ref[...])
            dp_ref[...] = jnp.concatenate([dm_ * ah_ref[...] * sh * (1.0 - sh),
                                           dm_ * aa_ref[...] * sa * (1.0 - sa)], axis=1).astype(BF16)
            dmh_ref[...] = (dm_ * sh).astype(BF16)
            dma_ref[...] = (dm_ * sa).astype(BF16)

    lat = pl.BlockSpec((TM, D), lambda i: (jnp.maximum(i - 1, 0), 0))
    return _pcall(body, name="merge_bwd", grid=(tt // TM,),
                  in_specs=[lat, lat, lat, pl.BlockSpec((TM, D), lambda i: (i, 2)),
                            pl.BlockSpec((TM, D), lambda i: (i, 3))],
                  out_specs=[pl.BlockSpec((TM, 2 * D), lambda i: (i, C_GATES)), lat, lat],
                  out_shape=[jax.ShapeDtypeStruct((tt, NCOL), BF16), jax.ShapeDtypeStruct((s_len, D), BF16),
                             jax.ShapeDtypeStruct((s_len, D), BF16)])(dm, ah, aa, p, p)


def _local_step(x, ctx, tgt, mod, modc, nw1, nw2, lg, hw, qnw, knw, sinks,
                w_in, bh4, ba4, w_o, g4, u4, dn4):
    s_len = x.shape[0]
    tt = s_len + L
    tok = jnp.concatenate([ctx, x], axis=0)
    ss1 = jnp.stack([modc, mod[0:2]])
    ss2 = mod[3:5][None]
    g1, g2 = mod[2:3], mod[5:6]
    hw4 = jnp.tile(hw, (1, 4))
    qnw8 = jnp.tile(qnw, (1, 8))
    knw2 = jnp.tile(knw, (1, 2))
    cos, sin = _rope_tables(s_len)
    bd512, bd128 = _blockdiag(ATW, HDIM), _blockdiag(128, HDIM)
    dupm = _dup_matrix()
    dup, dupt = jnp.asarray(dupm, BF16), jnp.asarray(dupm.T, F32)
    tmt = tt // 2 if tt % 512 else 512

    h = _modulate(tok, nw1, ss1, name="mod1", sel=lambda i: jnp.minimum(i, 1))
    p = _mm_in(h, w_in, tmt)
    o0, st0 = _hgrn_fwd(p, lg, rev=False)
    o1, st1 = _hgrn_fwd(p, lg, rev=True)
    y_hg = _readout(o0, o1, p, hw4)
    qr, k4, v4 = _qk_prep(p, cos, sin, qnw8, knw2, bd512, bd128, dup)
    y_at, lse = _attn_fwd(qr, k4, v4, sinks)
    ah = _mm_cs(y_hg, bh4, name="mm_bh")
    aa = _mm_cs(y_at, ba4, name="mm_ba")
    mixed = _merge(ah, aa, p)
    ao = _mm(mixed, w_o, name="mm_o", tm=512, tn=D, tk=D)
    x1, h2 = _res1_mod2(x, ao, g1, nw2, ss2)
    a4, b4, z4 = _ffn_up(h2, g4, u4)
    y = _ffn_down(z4, dn4)
    sq, dx2, dyb, dg2 = _loss_head(x1, y, g2, tgt)

    da4, db4 = _ffn_dz(dyb, dn4, a4, b4)
    g_dn = _ffn_gdn(z4, dyb)
    dh2 = _ffn_dh2(da4, db4, g4, u4)
    g_g, g_u = _ffn_ggu(h2, da4, db4)
    dx1, dattn, dss2, dnw2, dg1 = _mod2_bwd(x1, dh2, dx2, ao, nw2, ss2, g1)
    dm = _mm(dattn, w_o, name="mm_dm", mode="nt", tm=512, tn=D, tk=D)
    g_o = _mm(mixed, dattn, name="mm_go", mode="tn", tm=D, tn=D, tk=512)
    dp, dmh, dma = _merge_bwd(dm, ah, aa, p)
    dy_hg = _mm_cs_nt(dmh, bh4, name="mm_dyh")
    dy_at = _mm_cs_nt(dma, ba4, name="mm_dya")
    g_bh = _mm_cs_tn(y_hg, dmh, D // 4, name="mm_gbh")
    g_ba = _mm_cs_tn(y_at, dma, D // 4, name="mm_gba")
    dp, do, dhw4 = _readout_bwd(o0, o1, p, hw4, dy_hg, dp)
    dq, dkw, dvw, dkc, dvc, dsk = _attn_bwd(qr, k4, v4, sinks, y_at, lse, dy_at)
    dp, dqnw8, dknw2 = _attn_post(p, cos, sin, qnw8, knw2, bd512, bd128, dupt, dq, dkw, dvw, dkc, dvc, dp)
    dp, dv0, dq0, dlg0 = _hgrn_bwd(p, lg, do, st0, dp, None, rev=False)
    dp, dlg1 = _hgrn_bwd(p, lg, do, st1, dp, (dv0, dq0), rev=True)
    dh = _mm_dh(dp, w_in, tmt)
    g_in = _mm_gin(dp, h, tmt)
    gx, dss1, dnw1 = _mod1_bwd(tok, dh, dx1, nw1, ss1)

    dmod = jnp.concatenate([dss1[1], dg1, dss2, dg2], axis=0)
    dmodc = dss1[0]
    raw = (dss1, dg1, dss2, dg2, dnw1, dnw2, dhw4, dqnw8, dknw2, dsk, dlg0, dlg1)
    small = dict(raw=raw, dmod=dmod, dmodc=dmodc, dnw1=dnw1, dnw2=dnw2,
                 dhw=dhw4.reshape(4, HGD).sum(0, keepdims=True),
                 dqnw=dqnw8.reshape(8, HDIM).sum(0, keepdims=True),
                 dknw=dknw2.reshape(2, HDIM).sum(0, keepdims=True),
                 dsinks=dsk[:, 0], dlg=jnp.concatenate([dlg0, dlg1], axis=0))
    big = dict(w_in=g_in, w_bh=g_bh, w_ba=g_ba, w_o=g_o, w_g=g_g, w_u=g_u, w_dn=g_dn)
    return sq, gx, big, small


def _place():
    x, y, c = lax.axis_index("x"), lax.axis_index("y"), lax.axis_index("c")
    return x, y, c


def _gather_blocks(x_refs, out_refs, send_sems, recv_sems, local_sems):
    n = len(out_refs)
    x, y, c = _place()
    me, sibling = (x, y, c), (x, y, 1 - c)
    chips = [(1 - x, y), (x, 1 - y), (1 - x, 1 - y)]

    def slot(u, px, py, pc):
        return out_refs[u].at[4 * px + 2 * py + pc]

    def copy(u, k, block, to, src=None):
        return pltpu.make_async_remote_copy(
            src_ref=slot(u, *block) if src is None else src, dst_ref=slot(u, *block),
            send_sem=send_sems.at[u, k], recv_sem=recv_sems.at[u, k], device_id=to, device_id_type=MESH)

    mines = []
    if x_refs is not None:
        mines = [pltpu.make_async_copy(x_refs[u], slot(u, *me), local_sems.at[u]) for u in range(n)]
    for cp in mines:
        cp.start()
    first = []
    for u in range(n):
        src = None if x_refs is None else x_refs[u]
        first.append(copy(u, 0, me, sibling, src=src))
        first += [copy(u, 1 + j, me, (*chip, c), src=src) for j, chip in enumerate(chips)]
    for cp in first:
        cp.start()
    passed = []
    for j, chip in enumerate(chips):
        for u in range(n):
            copy(u, 1 + j, (*chip, c), me).wait_recv()
            fwd = copy(u, 4 + j, (*chip, c), sibling)
            fwd.start()
            passed.append(fwd)
    for u in range(n):
        copy(u, 0, sibling, me).wait_recv()
    for j, chip in enumerate(chips):
        for u in range(n):
            copy(u, 4 + j, (*chip, 1 - c), me).wait_recv()
    for cp in first + passed:
        cp.wait_send()
    for cp in mines:
        cp.wait()


def _gather_sems(n):
    return [pltpu.SemaphoreType.DMA((n, 7)), pltpu.SemaphoreType.DMA((n, 7)), pltpu.SemaphoreType.DMA((n,))]


def _allgather(blks, *, name, in_vmem):
    n = len(blks)
    space = pltpu.VMEM if in_vmem else pl.ANY

    def body(*refs):
        _gather_blocks(refs[:n], refs[n:2 * n], *refs[2 * n:])

    return pl.pallas_call(
        body, name=name, out_shape=[jax.ShapeDtypeStruct((8,) + b.shape, b.dtype) for b in blks],
        in_specs=[pl.BlockSpec(memory_space=space)] * n, out_specs=[pl.BlockSpec(memory_space=space)] * n,
        scratch_shapes=_gather_sems(n))(*blks)


def _cast_place(ws, c, dev):
    n = len(ws)

    def body(s_ref, *refs):
        for u in range(n):
            refs[n + u][0] = refs[u][...].astype(BF16)

    in_specs, out_specs, out_shape = [], [], []
    for w in ws:
        q, cols = w.shape[0] // 4, w.shape[1]
        in_specs.append(pl.BlockSpec((q, cols), lambda i, s: (2 * s[0] + i, 0)))
        out_specs.append(pl.BlockSpec((1, q, cols), lambda i, s: (s[1], i, 0)))
        out_shape.append(jax.ShapeDtypeStruct((8, 2 * q, cols), BF16))
    return pl.pallas_call(
        body, name="cast_place",
        grid_spec=pltpu.PrefetchScalarGridSpec(num_scalar_prefetch=1, grid=(2,), in_specs=in_specs,
                                               out_specs=out_specs),
        out_shape=out_shape,
        compiler_params=pltpu.CompilerParams(vmem_limit_bytes=48 << 20))(jnp.stack([c, dev]), *ws)


def _allgather_inplace(bufs, *, name):
    n = len(bufs)

    def body(*refs):
        _gather_blocks(None, refs[n:2 * n], *refs[2 * n:])

    return pl.pallas_call(
        body, name=name, out_shape=[jax.ShapeDtypeStruct(b.shape, b.dtype) for b in bufs],
        in_specs=[ANY] * n, out_specs=[ANY] * n, input_output_aliases={u: u for u in range(n)},
        scratch_shapes=_gather_sems(n))(*bufs)


def _ag_small(raw):
    def body(dss1, dg1, dss2, dg2, dnw1, dnw2, dhw4, dqnw8, dknw2, dsk, dlg0, dlg1,
             out_ref, tot_ref, blk, send_sems, recv_sems, local_sems):
        blk[...] = jnp.zeros_like(blk)
        blk[0:2, :] = dss1[1]
        blk[2:3, :] = dg1[...]
        blk[3:5, :] = dss2[...]
        blk[5:6, :] = dg2[...]
        blk[6:8, :] = dss1[0]
        blk[8:9, :] = dnw1[...]
        blk[9:10, :] = dnw2[...]
        blk[10:11, 0:HGW] = dhw4[...]
        blk[10:11, HGW:D] = dqnw8[...]
        blk[11:12, 0:128] = dknw2[...]
        blk[12:14, 0:HGW] = dlg0[0]
        blk[14:16, 0:HGW] = dlg1[0]
        blk[16:24, 0:128] = dsk[...]
        _gather_blocks([blk], [out_ref], send_sems, recv_sems, local_sems)
        acc = out_ref[0]
        for i in range(1, 8):
            acc = acc + out_ref[i]
        tot_ref[...] = acc

    vm = pl.BlockSpec(memory_space=pltpu.VMEM)
    return pl.pallas_call(
        body, name="ag_small",
        out_shape=[jax.ShapeDtypeStruct((8, 24, D), F32), jax.ShapeDtypeStruct((24, D), F32)],
        in_specs=[vm] * 12, out_specs=[vm, vm],
        scratch_shapes=[pltpu.VMEM((24, D), F32)] + _gather_sems(1))(*raw)


def _rs_pair_exchange(units):
    n = len(units)

    def body(*refs):
        g_refs, r_refs = refs[:n], refs[n:2 * n]
        send_sems, recv_sems = refs[2 * n:]
        x, y, c = _place()
        cps = [pltpu.make_async_remote_copy(
            src_ref=g_refs[u].at[j, 1 - c], dst_ref=r_refs[u].at[j], send_sem=send_sems.at[u, j],
            recv_sem=recv_sems.at[u, j], device_id=(x, y, 1 - c), device_id_type=MESH)
            for u in range(n) for j in range(4)]
        for cp in cps:
            cp.start()
        for cp in cps:
            cp.wait()

    return pl.pallas_call(
        body, name="rs_pair_exchange",
        out_shape=[jax.ShapeDtypeStruct((4,) + g.shape[2:], g.dtype) for g in units],
        in_specs=[ANY] * n, out_specs=[ANY] * n,
        scratch_shapes=[pltpu.SemaphoreType.DMA((n, 4)), pltpu.SemaphoreType.DMA((n, 4))])(*units)


def _rs_pair_add(units, recvs, c):
    n = len(units)

    def body(c_ref, *refs):
        for u in range(n):
            refs[2 * n + u][...] = (refs[u][0] + refs[n + u][...]).astype(BF16)

    in_specs, out_specs, out_shape = [], [], []
    for g in units:
        h, w = g.shape[2] // 2, g.shape[3]
        in_specs.append(pl.BlockSpec((1, 1, h, w), lambda j, i, cr: (j, cr[0], i, 0)))
    for g in units:
        h, w = g.shape[2] // 2, g.shape[3]
        in_specs.append(pl.BlockSpec((1, h, w), lambda j, i, cr: (j, i, 0)))
        out_specs.append(pl.BlockSpec((1, h, w), lambda j, i, cr: (j, i, 0)))
        out_shape.append(jax.ShapeDtypeStruct((4, 2 * h, w), BF16))
    return pl.pallas_call(
        body, name="rs_pair_add",
        grid_spec=pltpu.PrefetchScalarGridSpec(num_scalar_prefetch=1, grid=(4, 2), in_specs=in_specs,
                                               out_specs=out_specs),
        out_shape=out_shape,
        compiler_params=pltpu.CompilerParams(vmem_limit_bytes=48 << 20))(c.reshape(1), *units, *recvs)


def _rs_chip_exchange(pairs):
    n = len(pairs)

    def body(*refs):
        p_refs, r_refs = refs[:n], refs[n:2 * n]
        send_sems, recv_sems = refs[2 * n:]
        x, y, c = _place()
        k = 2 * x + y
        sends = []
        for d in range(1, 4):
            j = (k + d) % 4
            for u in range(n):
                sends.append(pltpu.make_async_remote_copy(
                    src_ref=p_refs[u].at[j], dst_ref=r_refs[u].at[k], send_sem=send_sems.at[u, d - 1],
                    recv_sem=recv_sems.at[u, d - 1], device_id=(j // 2, j % 2, c), device_id_type=MESH))
        for cp in sends:
            cp.start()
        for d in range(1, 4):
            src = (k + 4 - d) % 4
            for u in range(n):
                pltpu.make_async_remote_copy(
                    src_ref=p_refs[u].at[src], dst_ref=r_refs[u].at[src], send_sem=send_sems.at[u, d - 1],
                    recv_sem=recv_sems.at[u, d - 1], device_id=(x, y, c), device_id_type=MESH).wait_recv()
        for cp in sends:
            cp.wait_send()

    return pl.pallas_call(
        body, name="rs_chip_exchange", out_shape=[jax.ShapeDtypeStruct(p.shape, p.dtype) for p in pairs],
        in_specs=[ANY] * n, out_specs=[ANY] * n,
        scratch_shapes=[pltpu.SemaphoreType.DMA((n, 3)), pltpu.SemaphoreType.DMA((n, 3))])(*pairs)


def _rs_chip_add(pairs, contribs, c, chip):
    n = len(pairs)

    def body(s_ref, *refs):
        for u in range(n):
            a, b, c_, d = refs[4 * u:4 * u + 4]
            refs[4 * n + u][0] = ((a[0].astype(F32) + b[0].astype(F32)) + c_[0].astype(F32)) + d[0].astype(F32)

    in_specs, out_specs, out_shape, args = [], [], [], []
    for p, r in zip(pairs, contribs):
        h, w = p.shape[1] // 2, p.shape[2]
        in_specs += [pl.BlockSpec((1, h, w), functools.partial(lambda d, i, s: ((s[1] + d) % 4, i, 0), d))
                     for d in range(4)]
        args += [p, r, r, r]
        out_specs.append(pl.BlockSpec((1, h, w), lambda i, s: (s[0], i, 0)))
        out_shape.append(jax.ShapeDtypeStruct((2, 2 * h, w), F32))
    return pl.pallas_call(
        body, name="rs_chip_add",
        grid_spec=pltpu.PrefetchScalarGridSpec(num_scalar_prefetch=1, grid=(2,), in_specs=in_specs,
                                               out_specs=out_specs),
        out_shape=out_shape,
        compiler_params=pltpu.CompilerParams(vmem_limit_bytes=48 << 20))(jnp.stack([c, chip]), *args)


def _rs_sibling_gather(reds):
    n = len(reds)

    def body(*refs):
        o_refs = refs[n:2 * n]
        send_sems, recv_sems = refs[2 * n:]
        x, y, c = _place()
        cps = [pltpu.make_async_remote_copy(
            src_ref=o_refs[u].at[c], dst_ref=o_refs[u].at[c], send_sem=send_sems.at[u], recv_sem=recv_sems.at[u],
            device_id=(x, y, 1 - c), device_id_type=MESH) for u in range(n)]
        for cp in cps:
            cp.start()
        for u in range(n):
            cps[u].wait_send()
            pltpu.make_async_remote_copy(
                src_ref=o_refs[u].at[1 - c], dst_ref=o_refs[u].at[1 - c], send_sem=send_sems.at[u],
                recv_sem=recv_sems.at[u], device_id=(x, y, 1 - c), device_id_type=MESH).wait_recv()

    return pl.pallas_call(
        body, name="rs_sibling_gather", out_shape=[jax.ShapeDtypeStruct(r.shape, r.dtype) for r in reds],
        in_specs=[ANY] * n, out_specs=[ANY] * n, input_output_aliases={u: u for u in range(n)},
        scratch_shapes=[pltpu.SemaphoreType.DMA((n,))] * 2)(*reds)


def _ada_fwd(c16, w, b):
    n = w.shape[1]
    tn = 512

    def body(c_ref, w_ref, b_ref, o_ref):
        cc = c_ref[...]
        o_ref[...] = _dot(cc * _sig(cc), w_ref[...], prec=HI) + b_ref[...]

    return _pcall(body, name="ada_fwd", grid=(n // tn,),
                  in_specs=[_full((16, D)), pl.BlockSpec((D, tn), lambda j: (0, j)),
                            pl.BlockSpec((1, tn), lambda j: (0, j))],
                  out_specs=pl.BlockSpec((16, tn), lambda j: (0, j)),
                  out_shape=jax.ShapeDtypeStruct((16, n), F32))(c16, w, b)


def _ada_bwd(c16, dmod16, w):
    n = w.shape[1]
    tn = 512

    def body(c_ref, d_ref, w_ref, gw_ref, gc_ref):
        j = pl.program_id(0)

        @pl.when(j == 0)
        def _():
            gc_ref[...] = jnp.zeros_like(gc_ref)

        cc = c_ref[...]
        dm = d_ref[...]
        gw_ref[...] = _dot(cc * _sig(cc), dm, TN, prec=HI)
        gc_ref[...] += _dot(dm, w_ref[...], NT, prec=HI)

    return _pcall(body, name="ada_bwd", grid=(n // tn,),
                  in_specs=[_full((16, D)), pl.BlockSpec((16, tn), lambda j: (0, j)),
                            pl.BlockSpec((D, tn), lambda j: (0, j))],
                  out_specs=[pl.BlockSpec((D, tn), lambda j: (0, j)), _full((16, D))],
                  out_shape=[jax.ShapeDtypeStruct((D, n), F32),
                             jax.ShapeDtypeStruct((16, D), F32)])(c16, dmod16, w)


def _adam_math(w, g, m, v):
    c1 = 1.0 - ADAM_B1 ** ADAM_STEP
    c2 = 1.0 - ADAM_B2 ** ADAM_STEP
    nm = ADAM_B1 * m + (1.0 - ADAM_B1) * g
    nv = ADAM_B2 * v + (1.0 - ADAM_B2) * (g * g)
    return -ADAM_LR * ((nm / c1) / (jnp.sqrt(nv / c2) + ADAM_EPS) + ADAM_WD * w), nm, nv


def _adamw_small(ws, gs, ms, vs):
    n = len(ws)

    def body(*refs):
        for u in range(n):
            d_, nm, nv = _adam_math(refs[u][...], refs[n + u][...], refs[2 * n + u][...], refs[3 * n + u][...])
            refs[4 * n + u][...] = d_
            refs[5 * n + u][...] = nm
            refs[6 * n + u][...] = nv

    specs = [_full(w.shape) for w in ws]
    shapes = [jax.ShapeDtypeStruct(w.shape, F32) for w in ws]
    out = _pcall(body, name="adamw_small", grid=(1,), in_specs=specs * 4, out_specs=specs * 3,
                 out_shape=shapes * 3)(*ws, *gs, *ms, *vs)
    return out[:n], out[n:2 * n], out[2 * n:]


def _cctx_grad(parts, c_ctx):
    def body(p_ref, c_ref, o_ref):
        acc = p_ref[0:1, :]
        for k in range(1, 4):
            acc = acc + p_ref[k:k + 1, :]
        cc = c_ref[...]
        s = _sig(cc)
        o_ref[...] = acc * (s * (1.0 + cc * (1.0 - s)))

    return _pcall(body, name="cctx_grad", grid=(1,), in_specs=[_full(parts.shape), _full((1, D))],
                  out_specs=_full((1, D)), out_shape=jax.ShapeDtypeStruct((1, D), F32))(parts, c_ctx)


def _adamw(w, g, m, v, *, name):
    rows, cols = w.shape
    tr = next((t for t in (256, 128, 64) if rows % t == 0), rows)

    def body(w_ref, g_ref, m_ref, v_ref, d_ref, nm_ref, nv_ref):
        d_ref[...], nm_ref[...], nv_ref[...] = _adam_math(w_ref[...], g_ref[...], m_ref[...], v_ref[...])

    spec = pl.BlockSpec((tr, cols), lambda i: (i, 0))
    out = jax.ShapeDtypeStruct((rows, cols), F32)
    return _pcall(body, name=name, grid=(rows // tr,), in_specs=[spec] * 4, out_specs=[spec] * 3,
                  out_shape=[out] * 3)(w, g, m, v)


def kernel(x, c, ctx, c_ctx, w_ada, b_ada, norm_mix_w, norm_ffn_w, w_in, hgrn_lb_logits, hgrn_norm_w, q_norm_w, k_norm_w, attn_sinks, w_branch_hgrn, w_branch_attn, w_out, w_ffn_gate, w_ffn_up, w_ffn_down, loss_target, m_c_ctx, m_w_ada, m_b_ada, m_norm_mix_w, m_norm_ffn_w, m_w_in, m_hgrn_lb_logits, m_hgrn_norm_w, m_q_norm_w, m_k_norm_w, m_attn_sinks, m_w_branch_hgrn, m_w_branch_attn, m_w_out, m_w_ffn_gate, m_w_ffn_up, m_w_ffn_down, v_c_ctx, v_w_ada, v_b_ada, v_norm_mix_w, v_norm_ffn_w, v_w_in, v_hgrn_lb_logits, v_hgrn_norm_w, v_q_norm_w, v_k_norm_w, v_attn_sinks, v_w_branch_hgrn, v_w_branch_attn, v_w_out, v_w_ffn_gate, v_w_ffn_up, v_w_ffn_down):
    xi, yi, ci = _place()
    chip = 2 * xi + yi
    dev = 2 * chip + ci
    s_len = x.shape[1]

    lbrow = jnp.pad(hgrn_lb_logits.reshape(1, 512), ((0, 0), (0, D - 512)))
    blk = jnp.concatenate([c, lbrow, jnp.zeros((6, D), F32)], axis=0)
    g0, = _allgather([blk], name="ag_cond", in_vmem=True)
    c16 = jnp.concatenate([g0[:, 0], c_ctx[None], jnp.zeros((7, D), F32)], axis=0)
    lg = g0[0::2, 1, :512].reshape(4, 2, 2, 128).transpose(1, 2, 0, 3).reshape(2, 2, HGW)

    nada = w_ada.shape[2]
    b_sh = lax.dynamic_slice(b_ada, (0, chip * nada), (1, nada))
    mod_sh = _ada_fwd(c16, w_ada[0], b_sh)
    g1, = _allgather([mod_sh], name="ag_mod", in_vmem=True)
    modall = g1[0::2].transpose(1, 0, 2).reshape(16, 4 * nada)
    mod = lax.dynamic_slice(modall, (dev, 0), (1, 6 * D)).reshape(6, D)
    modc = modall[8].reshape(6, D)[:2]

    shards = [w_in[0].T, w_branch_hgrn[0], w_branch_attn[0], w_out[0], w_ffn_gate[0].T, w_ffn_up[0].T,
              w_ffn_down[0]]
    gathered = _allgather_inplace(_cast_place(shards, ci, dev), name="ag_weights")
    in4, bh4, ba4, o4, g4, u4, dn4 = [g.reshape(4, 2 * g.shape[1], g.shape[2]) for g in gathered]

    sq, gx, big, small = _local_step(
        x[0], ctx[0], loss_target[0], mod, modc, norm_mix_w, norm_ffn_w, lg, hgrn_norm_w, q_norm_w,
        k_norm_w, attn_sinks[0], in4.reshape(NCOL, D), bh4, ba4, o4.reshape(D, D), g4, u4, dn4)
    loss = lax.psum(0.5 * jnp.sum(sq) / D, ("x", "y", "c"))

    units = [big["w_in"].reshape(4, NCOL // 4, D), big["w_bh"], big["w_ba"], big["w_o"].reshape(4, D // 4, D),
             big["w_g"], big["w_u"], big["w_dn"]]
    units = [u.reshape(4, 2, u.shape[1] // 2, u.shape[2]) for u in units]
    recvs = _rs_pair_exchange(units)
    pairs = _rs_pair_add(units, recvs, ci)
    contribs = _rs_chip_exchange(pairs)
    reds = _rs_chip_add(pairs, contribs, ci, chip)
    gsh = [r.reshape(2 * r.shape[1], r.shape[2]) for r in _rs_sibling_gather(reds)]

    g2, tot = _ag_small(small["raw"])
    dmodc_tot = jnp.pad(tot[6:8].reshape(1, 2 * D), ((0, 0), (0, 4 * D)))
    g_b_ada = tot[0:6].reshape(1, 6 * D) + dmodc_tot
    dmod16 = jnp.concatenate([g2[:, 0:6].reshape(8, 6 * D), dmodc_tot, jnp.zeros((7, 6 * D), F32)], axis=0)
    g_w_ada, gc_part = _ada_bwd(c16, lax.dynamic_slice(dmod16, (0, chip * nada), (16, nada)), w_ada[0])
    g3, = _allgather([gc_part[8:16]], name="ag_cctx", in_vmem=True)
    g_c_ctx = _cctx_grad(g3[0::2, 0], c_ctx[None])[0]
    g_nw1 = tot[8:9]
    g_nw2 = tot[9:10]
    g_hw = tot[10, :HGW].reshape(4, HGD).sum(0, keepdims=True)
    g_qnw = tot[10, HGW:].reshape(8, HDIM).sum(0, keepdims=True)
    g_knw = tot[11, :128].reshape(2, HDIM).sum(0, keepdims=True)
    g_sinks = tot[16:24, 0][None]
    g_lg = lax.dynamic_slice(tot[12:16, :HGW].reshape(2, 2, HGW), (0, 0, chip * 128), (2, 2, 128))

    names = ["c_ctx", "w_ada", "b_ada", "norm_mix_w", "norm_ffn_w", "w_in", "hgrn_lb_logits", "hgrn_norm_w",
             "q_norm_w", "k_norm_w", "attn_sinks", "w_branch_hgrn", "w_branch_attn", "w_out", "w_ffn_gate",
             "w_ffn_up", "w_ffn_down"]
    ws = dict(zip(names, [c_ctx, w_ada, b_ada, norm_mix_w, norm_ffn_w, w_in, hgrn_lb_logits, hgrn_norm_w,
                          q_norm_w, k_norm_w, attn_sinks, w_branch_hgrn, w_branch_attn, w_out, w_ffn_gate,
                          w_ffn_up, w_ffn_down]))
    ms = dict(zip(names, [m_c_ctx, m_w_ada, m_b_ada, m_norm_mix_w, m_norm_ffn_w, m_w_in, m_hgrn_lb_logits,
                          m_hgrn_norm_w, m_q_norm_w, m_k_norm_w, m_attn_sinks, m_w_branch_hgrn,
                          m_w_branch_attn, m_w_out, m_w_ffn_gate, m_w_ffn_up, m_w_ffn_down]))
    vs = dict(zip(names, [v_c_ctx, v_w_ada, v_b_ada, v_norm_mix_w, v_norm_ffn_w, v_w_in, v_hgrn_lb_logits,
                          v_hgrn_norm_w, v_q_norm_w, v_k_norm_w, v_attn_sinks, v_w_branch_hgrn,
                          v_w_branch_attn, v_w_out, v_w_ffn_gate, v_w_ffn_up, v_w_ffn_down]))
    grads = dict(c_ctx=g_c_ctx, w_ada=g_w_ada[None], b_ada=g_b_ada, norm_mix_w=g_nw1, norm_ffn_w=g_nw2,
                 w_in=gsh[0].T[None], hgrn_lb_logits=g_lg, hgrn_norm_w=g_hw, q_norm_w=g_qnw, k_norm_w=g_knw,
                 attn_sinks=g_sinks, w_branch_hgrn=gsh[1][None], w_branch_attn=gsh[2][None], w_out=gsh[3][None],
                 w_ffn_gate=gsh[4].T[None], w_ffn_up=gsh[5].T[None], w_ffn_down=gsh[6][None])
    big_names = ["w_ada", "w_in", "w_branch_hgrn", "w_branch_attn", "w_out", "w_ffn_gate", "w_ffn_up", "w_ffn_down"]
    transposed = {"w_in": gsh[0], "w_ffn_gate": gsh[4], "w_ffn_up": gsh[5]}
    small_names = [n for n in names if n not in big_names]
    delta, new_m, new_v = {}, {}, {}
    for n in big_names:
        if n in transposed:
            d_, m_, v_ = _adamw(ws[n][0].T, transposed[n], ms[n][0].T, vs[n][0].T, name="adamw_" + n)
            delta[n], new_m[n], new_v[n] = d_.T[None], m_.T[None], v_.T[None]
        else:
            d_, m_, v_ = _adamw(ws[n][0], grads[n][0], ms[n][0], vs[n][0], name="adamw_" + n)
            delta[n], new_m[n], new_v[n] = d_[None], m_[None], v_[None]

    def two_d(a):
        return a.reshape(1, -1) if a.ndim == 1 else a

    sd, sm_, sv = _adamw_small(*[[two_d(d[n]) for n in small_names] for d in (ws, grads, ms, vs)])
    for i, n in enumerate(small_names):
        for dst, src in ((delta, sd), (new_m, sm_), (new_v, sv)):
            dst[n] = src[i].reshape(ws[n].shape)
    return (loss, gx[None], *[grads[n] for n in names], *[delta[n] for n in names],
            *[new_m[n] for n in names], *[new_v[n] for n in names])
```

```python
import functools

import numpy as np
import jax
import jax.numpy as jnp
from jax import lax
from jax.experimental import pallas as pl
from jax.experimental.pallas import tpu as pltpu

F32 = jnp.float32
BF16 = jnp.bfloat16
HI = lax.Precision.HIGHEST
MESH = pl.DeviceIdType.MESH

D = 1024
L = 256
TM = 256
HGW = 512
HGD = 128
CH = 32
ATW = 512
HDIM = 64
BLK = 128
GRID_W = 64
DFF = 2816
NCOL = 5376
EPS = 1e-6
ROPE_THETA = 10000.0

C_FB, C_INP, C_QHG, C_FF = 0, 1, 2, 3
C_GATES = 1
C_GHG, C_QRAW = 8, 9
C_KV = 20
C_QKV = 6

ADAM_LR, ADAM_B1, ADAM_B2, ADAM_EPS, ADAM_WD, ADAM_STEP = 0.001, 0.9, 0.999, 1e-08, 0.01, 10

NN = (((1,), (0,)), ((), ()))
NT = (((1,), (1,)), ((), ()))
TN = (((0,), (0,)), ((), ()))


def _dot(a, b, dims=NN, prec=None):
    return lax.dot_general(a, b, dims, precision=prec, preferred_element_type=F32)


def _bdot(a, b, dims=NN):
    return _dot(a.astype(BF16), b.astype(BF16), dims)


def _sig(x):
    return 1.0 / (1.0 + jnp.exp(-x))


class _Carry:
    def __init__(self, ins, outs, aliases, scratch, phases):
        self.ins, self.outs, self.aliases, self.scratch, self.phases = ins, outs, aliases, scratch, phases


def _pcall(body, *, name, grid, in_specs, out_specs, out_shape, scratch=(), aliases=None, vmem_mb=48,
           carry=None):
    params = pltpu.CompilerParams(dimension_semantics=("arbitrary",) * len(grid),
                                  vmem_limit_bytes=vmem_mb << 20)
    if carry is None:
        return pl.pallas_call(
            body, name=name, grid=grid, in_specs=in_specs, out_specs=out_specs, out_shape=out_shape,
            scratch_shapes=list(scratch), input_output_aliases=aliases or {}, compiler_params=params)
    single = not isinstance(out_shape, (list, tuple))
    out_specs_l = [out_specs] if single else list(out_specs)
    out_shape_l = [out_shape] if single else list(out_shape)
    n_in, n_out, n_sc = len(in_specs), len(out_shape_l), len(scratch)
    k_in, k_out = len(carry.ins), len(carry.outs)
    nsteps = int(np.prod(grid))
    assert nsteps >= 3

    def wrapped(*refs):
        ins, cins = refs[:n_in], refs[n_in:n_in + k_in]
        o0 = n_in + k_in
        outs, couts = refs[o0:o0 + n_out], refs[o0 + n_out:o0 + n_out + k_out]
        s0 = o0 + n_out + k_out
        sc, csc = refs[s0:s0 + n_sc], refs[s0 + n_sc:]
        step = pl.program_id(0)
        for ax in range(1, len(grid)):
            step = step * grid[ax] + pl.program_id(ax)
        start, mid, end = carry.phases(cins, couts, csc)
        pl.when(step == 0)(start)
        body(*ins, *outs, *sc)
        if mid is not None:
            pl.when(step == nsteps // 2)(mid)
        pl.when(step == nsteps - 1)(end)

    all_aliases = dict(aliases or {})
    all_aliases.update({n_in + i: n_out + o for i, o in carry.aliases.items()})
    call = pl.pallas_call(
        wrapped, name=name, grid=grid, in_specs=list(in_specs) + [ANY] * k_in,
        out_specs=out_specs_l + [ANY] * k_out, out_shape=out_shape_l + list(carry.outs),
        scratch_shapes=list(scratch) + list(carry.scratch), input_output_aliases=all_aliases,
        compiler_params=params)

    def run(*args):
        res = call(*args, *carry.ins)
        core = res[:n_out]
        return (core[0] if single else list(core)), list(res[n_out:])

    return run


def _full(shape):
    nd = len(shape)
    return pl.BlockSpec(shape, lambda *_: (0,) * nd)


ANY = pl.BlockSpec(memory_space=pl.ANY)


def _mm(a, b, *, name, mode="nn", out_dtype=F32, tm, tn, tk):
    if mode == "nn":
        (m, k), (k2, n) = a.shape, b.shape
    elif mode == "nt":
        (m, k), (n, k2) = a.shape, b.shape
    else:
        (k, m), (k2, n) = a.shape, b.shape
    assert k == k2 and m % tm == 0 and n % tn == 0 and k % tk == 0, (name, a.shape, b.shape)
    nk = k // tk
    dims = {"nn": NN, "nt": NT, "tn": TN}[mode]

    def body(a_ref, b_ref, o_ref, acc):
        kk = pl.program_id(2)

        @pl.when(kk == 0)
        def _():
            acc[...] = jnp.zeros_like(acc)

        acc[...] += _bdot(a_ref[...], b_ref[...], dims)

        @pl.when(kk == nk - 1)
        def _():
            o_ref[...] = acc[...].astype(out_dtype)

    a_spec = (pl.BlockSpec((tk, tm), lambda i, j, kk: (kk, i)) if mode == "tn"
              else pl.BlockSpec((tm, tk), lambda i, j, kk: (i, kk)))
    b_spec = (pl.BlockSpec((tn, tk), lambda i, j, kk: (j, kk)) if mode == "nt"
              else pl.BlockSpec((tk, tn), lambda i, j, kk: (kk, j)))
    return _pcall(body, name=name, grid=(m // tm, n // tn, nk), in_specs=[a_spec, b_spec],
                  out_specs=pl.BlockSpec((tm, tn), lambda i, j, kk: (i, j)),
                  out_shape=jax.ShapeDtypeStruct((m, n), out_dtype),
                  scratch=[pltpu.VMEM((tm, tn), F32)])(a, b)


NT_IN = NCOL // 256


def _src_block(j):
    return j + jnp.where(j < 4, 2, jnp.where(j < 6, 3, jnp.where(j < 8, -6, jnp.where(
        j < 16, 5, jnp.where(j < 20, -7, -14)))))


def _mm_in(h, wt, tm, carry=None):
    tt = h.shape[0]

    def body(h_ref, w_ref, o_ref):
        o_ref[...] = _bdot(h_ref[...], w_ref[...], NT)

    return _pcall(body, name="mm_in", grid=(tt // tm, NT_IN),
                  in_specs=[pl.BlockSpec((tm, D), lambda i, j: (i, 0)),
                            pl.BlockSpec((256, D), lambda i, j: (_src_block(j), 0))],
                  out_specs=pl.BlockSpec((tm, 256), lambda i, j: (i, j)),
                  out_shape=jax.ShapeDtypeStruct((tt, NCOL), F32), carry=carry)(h, wt)


def _mm_dh(dp, wt, tm):
    tt = dp.shape[0]

    def body(d_ref, w_ref, o_ref, acc):
        kk = pl.program_id(1)

        @pl.when(kk == 0)
        def _():
            acc[...] = jnp.zeros_like(acc)

        acc[...] += _bdot(d_ref[...], w_ref[...])

        @pl.when(kk == NT_IN - 1)
        def _():
            o_ref[...] = acc[...]

    return _pcall(body, name="mm_dh", grid=(tt // tm, NT_IN),
                  in_specs=[pl.BlockSpec((tm, 256), lambda i, kk: (i, kk)),
                            pl.BlockSpec((256, D), lambda i, kk: (_src_block(kk), 0))],
                  out_specs=pl.BlockSpec((tm, D), lambda i, kk: (i, 0)),
                  out_shape=jax.ShapeDtypeStruct((tt, D), F32), scratch=[pltpu.VMEM((tm, D), F32)])(dp, wt)


def _mm_gin(dp, h, tk):
    tt = dp.shape[0]
    nk = tt // tk

    def body(d_ref, h_ref, o_ref, acc):
        kk = pl.program_id(1)

        @pl.when(kk == 0)
        def _():
            acc[...] = jnp.zeros_like(acc)

        acc[...] += _bdot(d_ref[...], h_ref[...], TN)

        @pl.when(kk == nk - 1)
        def _():
            o_ref[...] = acc[...]

    return _pcall(body, name="mm_gin", grid=(NT_IN, nk),
                  in_specs=[pl.BlockSpec((tk, 256), lambda j, kk: (kk, j)),
                            pl.BlockSpec((tk, D), lambda j, kk: (kk, 0))],
                  out_specs=pl.BlockSpec((256, D), lambda j, kk: (_src_block(j), 0)),
                  out_shape=jax.ShapeDtypeStruct((NCOL, D), F32), scratch=[pltpu.VMEM((256, D), F32)])(dp, h)


def _modulate(xin, nw, ss, *, name, sel):
    rows = xin.shape[0]

    def body(x_ref, nw_ref, ss_ref, h_ref):
        x = x_ref[...]
        r = lax.rsqrt(jnp.mean(x * x, axis=-1, keepdims=True) + EPS)
        s = ss_ref[0]
        h_ref[...] = ((x * r * nw_ref[...]) * (1.0 + s[1:2]) + s[0:1]).astype(BF16)

    return _pcall(body, name=name, grid=(rows // TM,),
                  in_specs=[pl.BlockSpec((TM, D), lambda i: (i, 0)), _full((1, D)),
                            pl.BlockSpec((1, 2, D), lambda i: (sel(i), 0, 0))],
                  out_specs=pl.BlockSpec((TM, D), lambda i: (i, 0)),
                  out_shape=jax.ShapeDtypeStruct((rows, D), BF16))(xin, nw, ss)


def _norm_bwd_rows(x, dh, nw, scale):
    r = lax.rsqrt(jnp.mean(x * x, axis=-1, keepdims=True) + EPS)
    xh = x * r
    dxh = dh * ((1.0 + scale) * nw)
    dx = r * (dxh - xh * jnp.mean(dxh * xh, axis=-1, keepdims=True))
    return dx, xh


def _res1_mod2(x, ao, g1, nw2, ss2):
    s_len = x.shape[0]

    def body(x_ref, ao_ref, g_ref, nw_ref, ss_ref, x1_ref, h_ref):
        x1 = x_ref[...] + g_ref[...] * ao_ref[...]
        x1_ref[...] = x1
        r = lax.rsqrt(jnp.mean(x1 * x1, axis=-1, keepdims=True) + EPS)
        s = ss_ref[0]
        h_ref[...] = ((x1 * r * nw_ref[...]) * (1.0 + s[1:2]) + s[0:1]).astype(BF16)

    row = pl.BlockSpec((TM, D), lambda i: (i, 0))
    return _pcall(body, name="res1_mod2", grid=(s_len // TM,),
                  in_specs=[row, row, _full((1, D)), _full((1, D)), _full((1, 2, D))],
                  out_specs=[row, row],
                  out_shape=[jax.ShapeDtypeStruct((s_len, D), F32),
                             jax.ShapeDtypeStruct((s_len, D), BF16)])(x, ao, g1, nw2, ss2)


TS = 512


def _acc_call(body, *, name, grid, in_specs, out_specs, out_shape, acc_shapes, args):
    return _pcall(body, name=name, grid=grid, in_specs=in_specs, out_specs=out_specs, out_shape=out_shape,
                  scratch=[pltpu.VMEM(s, F32) for s in acc_shapes])(*args)


def _mm_cs(a, w4, *, name):
    m, k = a.shape
    _, _, ns = w4.shape

    def body(a_ref, w_ref, o_ref):
        o_ref[...] = _bdot(a_ref[...], w_ref[0])

    return _pcall(body, name=name, grid=(m // TS, 4),
                  in_specs=[pl.BlockSpec((TS, k), lambda i, j: (i, 0)),
                            pl.BlockSpec((1, k, ns), lambda i, j: (j, 0, 0))],
                  out_specs=pl.BlockSpec((TS, ns), lambda i, j: (i, j)),
                  out_shape=jax.ShapeDtypeStruct((m, 4 * ns), F32))(a, w4)


def _mm_cs_nt(a, w4, *, name):
    m = a.shape[0]
    _, k, ns = w4.shape

    def body(a_ref, w_ref, o_ref, acc):
        j = pl.program_id(1)

        @pl.when(j == 0)
        def _():
            acc[...] = jnp.zeros_like(acc)

        acc[...] += _bdot(a_ref[...], w_ref[0], NT)

        @pl.when(j == 3)
        def _():
            o_ref[...] = acc[...]

    return _acc_call(body, name=name, grid=(m // TS, 4),
                     in_specs=[pl.BlockSpec((TS, ns), lambda i, j: (i, j)),
                               pl.BlockSpec((1, k, ns), lambda i, j: (j, 0, 0))],
                     out_specs=pl.BlockSpec((TS, k), lambda i, j: (i, 0)),
                     out_shape=jax.ShapeDtypeStruct((m, k), F32), acc_shapes=[(TS, k)], args=(a, w4))


def _mm_cs_tn(a, b, ns, *, name):
    s_len, k = a.shape
    nk = s_len // TS

    def body(a_ref, b_ref, o_ref, acc):
        t = pl.program_id(1)

        @pl.when(t == 0)
        def _():
            acc[...] = jnp.zeros_like(acc)

        acc[...] += _bdot(a_ref[...], b_ref[...], TN)

        @pl.when(t == nk - 1)
        def _():
            o_ref[0] = acc[...]

    return _acc_call(body, name=name, grid=(4, nk),
                     in_specs=[pl.BlockSpec((TS, k), lambda j, t: (t, 0)),
                               pl.BlockSpec((TS, ns), lambda j, t: (t, j))],
                     out_specs=pl.BlockSpec((1, k, ns), lambda j, t: (j, 0, 0)),
                     out_shape=jax.ShapeDtypeStruct((4, k, ns), F32), acc_shapes=[(k, ns)], args=(a, b))


def _ffn_up(h2, g4, u4):
    s_len = h2.shape[0]
    ns = g4.shape[1]

    def body(h_ref, g_ref, u_ref, a_ref, b_ref, z_ref):
        h = h_ref[...]
        a = _bdot(h, g_ref[0], NT)
        b = _bdot(h, u_ref[0], NT)
        a_ref[0] = a
        b_ref[0] = b
        z_ref[0] = (a * _sig(a) * b).astype(BF16)

    w = pl.BlockSpec((1, ns, D), lambda i, j: (j, 0, 0))
    o = pl.BlockSpec((1, TS, ns), lambda i, j: (j, i, 0))
    f = jax.ShapeDtypeStruct((4, s_len, ns), F32)
    return _pcall(body, name="ffn_up", grid=(s_len // TS, 4),
                  in_specs=[pl.BlockSpec((TS, D), lambda i, j: (i, 0)), w, w], out_specs=[o, o, o],
                  out_shape=[f, f, jax.ShapeDtypeStruct((4, s_len, ns), BF16)])(h2, g4, u4)


def _ffn_down(z4, dn4):
    _, s_len, ns = z4.shape

    def body(z_ref, w_ref, o_ref, acc):
        j = pl.program_id(1)

        @pl.when(j == 0)
        def _():
            acc[...] = jnp.zeros_like(acc)

        acc[...] += _bdot(z_ref[0], w_ref[0])

        @pl.when(j == 3)
        def _():
            o_ref[...] = acc[...]

    return _acc_call(body, name="ffn_down", grid=(s_len // TS, 4),
                     in_specs=[pl.BlockSpec((1, TS, ns), lambda i, j: (j, i, 0)),
                               pl.BlockSpec((1, ns, D), lambda i, j: (j, 0, 0))],
                     out_specs=pl.BlockSpec((TS, D), lambda i, j: (i, 0)),
                     out_shape=jax.ShapeDtypeStruct((s_len, D), F32), acc_shapes=[(TS, D)], args=(z4, dn4))


def _ffn_dz(dyb, dn4, a4, b4):
    _, s_len, ns = a4.shape

    def body(dy_ref, w_ref, a_ref, b_ref, da_ref, db_ref):
        dz = _bdot(dy_ref[...], w_ref[0], NT)
        a = a_ref[0]
        s = _sig(a)
        da_ref[0] = (dz * b_ref[0] * (s * (1.0 + a * (1.0 - s)))).astype(BF16)
        db_ref[0] = (dz * (a * s)).astype(BF16)

    t = pl.BlockSpec((1, TS, ns), lambda i, j: (j, i, 0))
    o = jax.ShapeDtypeStruct((4, s_len, ns), BF16)
    return _pcall(body, name="ffn_dz", grid=(s_len // TS, 4),
                  in_specs=[pl.BlockSpec((TS, D), lambda i, j: (i, 0)),
                            pl.BlockSpec((1, ns, D), lambda i, j: (j, 0, 0)), t, t],
                  out_specs=[t, t], out_shape=[o, o])(dyb, dn4, a4, b4)


def _ffn_gdn(z4, dyb):
    _, s_len, ns = z4.shape
    nk = s_len // TS

    def body(z_ref, dy_ref, o_ref, acc):
        t = pl.program_id(1)

        @pl.when(t == 0)
        def _():
            acc[...] = jnp.zeros_like(acc)

        acc[...] += _bdot(z_ref[0], dy_ref[...], TN)

        @pl.when(t == nk - 1)
        def _():
            o_ref[0] = acc[...]

    return _acc_call(body, name="ffn_gdn", grid=(4, nk),
                     in_specs=[pl.BlockSpec((1, TS, ns), lambda j, t: (j, t, 0)),
                               pl.BlockSpec((TS, D), lambda j, t: (t, 0))],
                     out_specs=pl.BlockSpec((1, ns, D), lambda j, t: (j, 0, 0)),
                     out_shape=jax.ShapeDtypeStruct((4, ns, D), F32), acc_shapes=[(ns, D)], args=(z4, dyb))


def _ffn_dh2(da4, db4, g4, u4):
    _, s_len, ns = da4.shape

    def body(da_ref, db_ref, g_ref, u_ref, o_ref, acc):
        j = pl.program_id(1)

        @pl.when(j == 0)
        def _():
            acc[...] = jnp.zeros_like(acc)

        acc[...] += _bdot(da_ref[0], g_ref[0]) + _bdot(db_ref[0], u_ref[0])

        @pl.when(j == 3)
        def _():
            o_ref[...] = acc[...]

    t = pl.BlockSpec((1, TS, ns), lambda i, j: (j, i, 0))
    w = pl.BlockSpec((1, ns, D), lambda i, j: (j, 0, 0))
    return _acc_call(body, name="ffn_dh2", grid=(s_len // TS, 4), in_specs=[t, t, w, w],
                     out_specs=pl.BlockSpec((TS, D), lambda i, j: (i, 0)),
                     out_shape=jax.ShapeDtypeStruct((s_len, D), F32), acc_shapes=[(TS, D)],
                     args=(da4, db4, g4, u4))


def _ffn_ggu(h2, da4, db4):
    _, s_len, ns = da4.shape
    nk = s_len // TS

    def body(h_ref, da_ref, db_ref, gg_ref, gu_ref, acc_g, acc_u):
        t = pl.program_id(1)

        @pl.when(t == 0)
        def _():
            acc_g[...] = jnp.zeros_like(acc_g)
            acc_u[...] = jnp.zeros_like(acc_u)

        h = h_ref[...]
        acc_g[...] += _bdot(da_ref[0], h, TN)
        acc_u[...] += _bdot(db_ref[0], h, TN)

        @pl.when(t == nk - 1)
        def _():
            gg_ref[0] = acc_g[...]
            gu_ref[0] = acc_u[...]

    d = pl.BlockSpec((1, TS, ns), lambda j, t: (j, t, 0))
    o = pl.BlockSpec((1, ns, D), lambda j, t: (j, 0, 0))
    f = jax.ShapeDtypeStruct((4, ns, D), F32)
    return _acc_call(body, name="ffn_ggu", grid=(4, nk),
                     in_specs=[pl.BlockSpec((TS, D), lambda j, t: (t, 0)), d, d], out_specs=[o, o],
                     out_shape=[f, f], acc_shapes=[(ns, D), (ns, D)], args=(h2, da4, db4))


def _loss_head(x1, y, g2, tgt):
    s_len = x1.shape[0]

    def body(x1_ref, y_ref, g_ref, t_ref, sq_ref, dx2_ref, dyb_ref, dg_ref):
        i = pl.program_id(0)

        @pl.when(i == 0)
        def _():
            sq_ref[...] = jnp.zeros_like(sq_ref)
            dg_ref[...] = jnp.zeros_like(dg_ref)

        y_ = y_ref[...]
        g = g_ref[...]
        e = x1_ref[...] + g * y_ - t_ref[...]
        sq_ref[...] += jnp.sum(e * e, axis=0, keepdims=True)
        dx2 = e * (1.0 / D)
        dx2_ref[...] = dx2
        dyb_ref[...] = (g * dx2).astype(BF16)
        dg_ref[...] += jnp.sum(dx2 * y_, axis=0, keepdims=True)

    row = pl.BlockSpec((TM, D), lambda i: (i, 0))
    vec = _full((1, D))
    return _pcall(body, name="loss_head", grid=(s_len // TM,),
                  in_specs=[row, row, vec, row], out_specs=[vec, row, row, vec],
                  out_shape=[jax.ShapeDtypeStruct((1, D), F32), jax.ShapeDtypeStruct((s_len, D), F32),
                             jax.ShapeDtypeStruct((s_len, D), BF16),
                             jax.ShapeDtypeStruct((1, D), F32)])(x1, y, g2, tgt)


def _mod2_bwd(x1, dh2, dx2, ao, nw2, ss2, g1):
    s_len = x1.shape[0]

    def body(x1_ref, dh_ref, dx2_ref, ao_ref, nw_ref, ss_ref, g_ref,
             dx1_ref, da_ref, dss_ref, dnw_ref, dg_ref):
        i = pl.program_id(0)

        @pl.when(i == 0)
        def _():
            dss_ref[...] = jnp.zeros_like(dss_ref)
            dnw_ref[...] = jnp.zeros_like(dnw_ref)
            dg_ref[...] = jnp.zeros_like(dg_ref)

        dh = dh_ref[...]
        nw = nw_ref[...]
        scale = ss_ref[0][1:2]
        dxn, xh = _norm_bwd_rows(x1_ref[...], dh, nw, scale)
        dx1 = dx2_ref[...] + dxn
        dx1_ref[...] = dx1
        da_ref[...] = (g_ref[...] * dx1).astype(BF16)
        dg_ref[...] += jnp.sum(dx1 * ao_ref[...], axis=0, keepdims=True)
        dsh = jnp.sum(dh, axis=0, keepdims=True)
        dsc = jnp.sum(dh * xh * nw, axis=0, keepdims=True)
        dss_ref[...] += jnp.concatenate([dsh, dsc], axis=0)
        dnw_ref[...] += jnp.sum(dh * xh * (1.0 + scale), axis=0, keepdims=True)

    row = pl.BlockSpec((TM, D), lambda i: (i, 0))
    vec = _full((1, D))
    return _pcall(body, name="mod2_bwd", grid=(s_len // TM,),
                  in_specs=[row, row, row, row, vec, _full((1, 2, D)), vec],
                  out_specs=[row, row, _full((2, D)), vec, vec],
                  out_shape=[jax.ShapeDtypeStruct((s_len, D), F32), jax.ShapeDtypeStruct((s_len, D), BF16),
                             jax.ShapeDtypeStruct((2, D), F32), jax.ShapeDtypeStruct((1, D), F32),
                             jax.ShapeDtypeStruct((1, D), F32)])(x1, dh2, dx2, ao, nw2, ss2, g1)


def _mod1_bwd(tok, dh, dx1, nw1, ss1, carry=None):
    tt = tok.shape[0]
    s_len = dx1.shape[0]

    def body(t_ref, dh_ref, dx1_ref, nw_ref, ss_ref, dx_ref, dss_ref, dnw_ref):
        i = pl.program_id(0)

        @pl.when(i == 0)
        def _():
            dnw_ref[...] = jnp.zeros_like(dnw_ref)

        @pl.when(i <= 1)
        def _():
            dss_ref[...] = jnp.zeros_like(dss_ref)

        dh_ = dh_ref[...]
        nw = nw_ref[...]
        scale = ss_ref[0][1:2]
        dxn, xh = _norm_bwd_rows(t_ref[...], dh_, nw, scale)

        @pl.when(i >= 1)
        def _():
            dx_ref[...] = dx1_ref[...] + dxn

        dsh = jnp.sum(dh_, axis=0, keepdims=True)
        dsc = jnp.sum(dh_ * xh * nw, axis=0, keepdims=True)
        dss_ref[...] += jnp.concatenate([dsh, dsc], axis=0)[None]
        dnw_ref[...] += jnp.sum(dh_ * xh * (1.0 + scale), axis=0, keepdims=True)

    row = pl.BlockSpec((TM, D), lambda i: (i, 0))
    lat = pl.BlockSpec((TM, D), lambda i: (jnp.maximum(i - 1, 0), 0))
    sel = pl.BlockSpec((1, 2, D), lambda i: (jnp.minimum(i, 1), 0, 0))
    return _pcall(body, name="mod1_bwd", grid=(tt // TM,),
                  in_specs=[row, row, lat, _full((1, D)), sel],
                  out_specs=[lat, sel, _full((1, D))],
                  out_shape=[jax.ShapeDtypeStruct((s_len, D), F32), jax.ShapeDtypeStruct((2, 2, D), F32),
                             jax.ShapeDtypeStruct((1, D), F32)], carry=carry)(tok, dh, dx1, nw1, ss1)


def _tri(rev, transpose=False):
    r = lax.broadcasted_iota(jnp.int32, (CH, CH), 0)
    c = lax.broadcasted_iota(jnp.int32, (CH, CH), 1)
    lower = (c >= r) if (rev != transpose) else (c <= r)
    return lower


def _hgrn_gate(fl, qraw, lg):
    lb = 1.0 / (1.0 + jnp.exp(lg[1:2] - lg[0:1]))
    sg = _sig(fl)
    f = lb + (1.0 - lb) * sg
    q = qraw * _sig(qraw) * (HGD ** -0.5)
    return lb, sg, f, q


def _hgrn_fwd(p, lg, *, rev, carry=None):
    tt = p.shape[0]
    nt = tt // TM
    ncht = TM // CH
    d = 1 if rev else 0

    def tile_of(s):
        return jnp.where(s == 0, 0, nt - s) if rev else s

    def body(f_ref, inp_ref, q_ref, lg_ref, o_ref, st_ref, lf_s, k_s, q_s, state):
        s = pl.program_id(0)

        @pl.when(s == 0)
        def _():
            state[...] = jnp.zeros_like(state)

        _, _, f, q = _hgrn_gate(f_ref[...], q_ref[...], lg_ref[0])
        lf_s[...] = jnp.log(f)
        k_s[...] = 1.0 - f
        q_s[...] = q
        tri = _tri(rev)
        trif = tri.astype(F32)

        def chunk(cc, carry):
            c = (ncht - 1 - cc) if rev else cc
            r0 = pl.multiple_of(c * CH, CH)
            lf = lf_s[pl.ds(r0, CH), :]
            cum = _dot(trif, lf, prec=HI)
            tot = jnp.sum(lf, axis=0, keepdims=True)
            qd = q_s[pl.ds(r0, CH), :] * jnp.exp(cum)
            kk = k_s[pl.ds(r0, CH), :]
            kd = kk * jnp.exp(-cum)
            ke = kk * jnp.exp(tot - cum)
            et = jnp.exp(tot)
            v = inp_ref[pl.ds(r0, CH), :]
            outs = []
            for h in range(4):
                sl = slice(h * HGD, (h + 1) * HGD)
                st0 = state[h]
                st_ref[c, h] = st0
                pm = jnp.where(tri, _bdot(qd[:, sl], kd[:, sl], NT), 0.0)
                outs.append(_bdot(pm, v[:, sl]) + _bdot(qd[:, sl], st0, NT))
                state[h] = st0 * et[:, sl] + _bdot(v[:, sl], ke[:, sl], TN)
            o_ref[pl.ds(r0, CH), :] = jnp.concatenate(outs, axis=1)
            return carry

        lax.fori_loop(0, ncht, chunk, 0)

    def col(cb):
        return pl.BlockSpec((TM, HGW), lambda s: (tile_of(s), cb))

    return _pcall(
        body, name="hgrn_fwd_rev" if rev else "hgrn_fwd", grid=(nt,),
        in_specs=[col(C_FB if rev else C_FF), col(C_INP), col(C_QHG),
                  pl.BlockSpec((1, 2, HGW), lambda s: (d, 0, 0))],
        out_specs=[pl.BlockSpec((TM, HGW), lambda s: (tile_of(s), 0)),
                   pl.BlockSpec((ncht, 4, HGD, HGD), lambda s: (tile_of(s), 0, 0, 0))],
        out_shape=[jax.ShapeDtypeStruct((tt, HGW), F32),
                   jax.ShapeDtypeStruct((nt * ncht, 4, HGD, HGD), F32)],
        scratch=[pltpu.VMEM((TM, HGW), F32)] * 3 + [pltpu.VMEM((4, HGD, HGD), F32)],
        carry=carry)(p, p, p, lg)


def _hgrn_bwd(p, lg, do, st, dp, prev, *, rev, carry=None):
    tt = p.shape[0]
    nt = tt // TM
    ncht = TM // CH
    d = 1 if rev else 0
    second = prev is not None

    def tile_of(s):
        return jnp.where(s == nt - 1, 0, s + 1) if rev else nt - 1 - s

    def body(*refs):
        if second:
            (f_ref, inp_ref, q_ref, lg_ref, do_ref, st_ref, dvp_ref, dqp_ref, _dp_in,
             dp_ref, dlg_ref, lf_s, k_s, q_s, do_s, dq_s, dk_s, dv_s, dlf_s, dstate) = refs
        else:
            (f_ref, inp_ref, q_ref, lg_ref, do_ref, st_ref, _dp_in,
             dp_ref, dv_ref, dq_ref, dlg_ref, lf_s, k_s, q_s, do_s, dq_s, dk_s, dv_s, dlf_s, dstate) = refs
        s = pl.program_id(0)
        tile = tile_of(s)

        @pl.when(s == 0)
        def _():
            dstate[...] = jnp.zeros_like(dstate)
            dlg_ref[...] = jnp.zeros_like(dlg_ref)

        qraw = q_ref[...]
        lb, sg, f, q = _hgrn_gate(f_ref[...], qraw, lg_ref[0])
        lf_s[...] = jnp.log(f)
        k_s[...] = 1.0 - f
        q_s[...] = q
        do_s[...] = jnp.where(tile == 0, 0.0, do_ref[...])
        tri = _tri(rev)
        trit = _tri(rev, transpose=True)
        tritf = trit.astype(F32)
        trif = tri.astype(F32)

        def chunk(cc, carry):
            c = cc if rev else (ncht - 1 - cc)
            r0 = pl.multiple_of(c * CH, CH)
            lf = lf_s[pl.ds(r0, CH), :]
            cum = _dot(trif, lf, prec=HI)
            tot = jnp.sum(lf, axis=0, keepdims=True)
            ea = jnp.exp(cum)
            eb = jnp.exp(-cum)
            ee = jnp.exp(tot - cum)
            et = jnp.exp(tot)
            qd = q_s[pl.ds(r0, CH), :] * ea
            kk = k_s[pl.ds(r0, CH), :]
            kd = kk * eb
            ke = kk * ee
            v = inp_ref[pl.ds(r0, CH), :]
            doc = do_s[pl.ds(r0, CH), :]
            dq_l, dk_l, dv_l, dcum_l, dtot_l = [], [], [], [], []
            for h in range(4):
                sl = slice(h * HGD, (h + 1) * HGD)
                qd_, kd_, ke_, v_, do_ = qd[:, sl], kd[:, sl], ke[:, sl], v[:, sl], doc[:, sl]
                st0 = st_ref[c, h]
                ds1 = dstate[h]
                pmt = jnp.where(trit, _bdot(kd_, qd_, NT), 0.0)
                dpm = jnp.where(tri, _bdot(do_, v_, NT), 0.0)
                dpmt = jnp.where(trit, _bdot(v_, do_, NT), 0.0)
                dv = _bdot(pmt, do_) + _bdot(ke_, ds1, NT)
                dqd = _bdot(dpm, kd_) + _bdot(do_, st0)
                dkd = _bdot(dpmt, qd_)
                dke = _bdot(v_, ds1)
                dstate[h] = ds1 * et[:, sl] + _bdot(do_, qd_, TN)
                dtot_l.append(jnp.sum(ds1 * st0, axis=0, keepdims=True) * et[:, sl]
                              + jnp.sum(dke * ke_, axis=0, keepdims=True))
                dq_l.append(dqd * ea[:, sl])
                dk_l.append(dkd * eb[:, sl] + dke * ee[:, sl])
                dv_l.append(dv)
                dcum_l.append(dqd * qd_ - dkd * kd_ - dke * ke_)
            dcum = jnp.concatenate(dcum_l, axis=1)
            dlf = _dot(tritf, dcum, prec=HI) + jnp.concatenate(dtot_l, axis=1)
            dq_s[pl.ds(r0, CH), :] = jnp.concatenate(dq_l, axis=1)
            dk_s[pl.ds(r0, CH), :] = jnp.concatenate(dk_l, axis=1)
            dv_s[pl.ds(r0, CH), :] = jnp.concatenate(dv_l, axis=1)
            dlf_s[pl.ds(r0, CH), :] = dlf
            return carry

        lax.fori_loop(0, ncht, chunk, 0)

        df = dlf_s[...] / f - dk_s[...]
        dfl = df * (1.0 - lb) * sg * (1.0 - sg)
        dlb = jnp.sum(df * (1.0 - sg), axis=0, keepdims=True)
        dl0 = dlb * lb * (1.0 - lb)
        dlg_ref[...] += jnp.concatenate([dl0, -dl0], axis=0)[None]
        if second:
            sq = _sig(qraw)
            dqr = (dqp_ref[...] + dq_s[...]) * (HGD ** -0.5) * (sq * (1.0 + qraw * (1.0 - sq)))
            dp_ref[...] = jnp.concatenate([dfl, dvp_ref[...] + dv_s[...], dqr], axis=1).astype(BF16)
        else:
            dp_ref[...] = dfl.astype(BF16)
            dv_ref[...] = dv_s[...]
            dq_ref[...] = dq_s[...]

    def col(cb):
        return pl.BlockSpec((TM, HGW), lambda s: (tile_of(s), cb))

    tok = pl.BlockSpec((TM, HGW), lambda s: (tile_of(s), 0))
    in_specs = [col(C_FB if rev else C_FF), col(C_INP), col(C_QHG),
                pl.BlockSpec((1, 2, HGW), lambda s: (d, 0, 0)),
                pl.BlockSpec((TM, HGW), lambda s: (jnp.maximum(tile_of(s) - 1, 0), 0)),
                pl.BlockSpec((ncht, 4, HGD, HGD), lambda s: (tile_of(s), 0, 0, 0))]
    args = [p, p, p, lg, do, st]
    dlg_spec = _full((1, 2, HGW))
    dlg_shape = jax.ShapeDtypeStruct((1, 2, HGW), F32)
    if second:
        in_specs += [tok, tok]
        args += [prev[0], prev[1]]
        out_specs = [pl.BlockSpec((TM, 3 * HGW), lambda s: (tile_of(s), 0)), dlg_spec]
        out_shape = [jax.ShapeDtypeStruct(dp.shape, BF16), dlg_shape]
    else:
        out_specs = [pl.BlockSpec((TM, HGW), lambda s: (tile_of(s), C_FB if rev else C_FF)), tok, tok, dlg_spec]
        out_shape = [jax.ShapeDtypeStruct(dp.shape, BF16), jax.ShapeDtypeStruct((tt, HGW), F32),
                     jax.ShapeDtypeStruct((tt, HGW), F32), dlg_shape]
    in_specs.append(ANY)
    args.append(dp)
    return _pcall(body, name="hgrn_bwd_rev" if rev else "hgrn_bwd", grid=(nt,),
                  in_specs=in_specs, out_specs=out_specs, out_shape=out_shape,
                  scratch=[pltpu.VMEM((TM, HGW), F32)] * 8 + [pltpu.VMEM((4, HGD, HGD), F32)],
                  aliases={len(args) - 1: 0}, carry=carry)(*args)


def _head_rms(o, w, nheads):
    outs = []
    for h in range(nheads):
        oh = o[:, h * HGD:(h + 1) * HGD]
        outs.append(oh * lax.rsqrt(jnp.mean(oh * oh, axis=-1, keepdims=True) + EPS))
    return jnp.concatenate(outs, axis=1)


def _readout(o0, o1, p, hw4):
    s_len = o0.shape[0] - L

    def body(o0_ref, o1_ref, g_ref, w_ref, y_ref):
        xh = _head_rms(o0_ref[...] + o1_ref[...], None, 4)
        g = g_ref[...]
        y_ref[...] = (xh * w_ref[...] * (g * _sig(g))).astype(BF16)

    lat = pl.BlockSpec((TM, HGW), lambda i: (i + 1, 0))
    return _pcall(body, name="readout", grid=(s_len // TM,),
                  in_specs=[lat, lat, pl.BlockSpec((TM, HGW), lambda i: (i + 1, C_GHG)), _full((1, HGW))],
                  out_specs=pl.BlockSpec((TM, HGW), lambda i: (i, 0)),
                  out_shape=jax.ShapeDtypeStruct((s_len, HGW), BF16))(o0, o1, p, hw4)


def _readout_bwd(o0, o1, p, hw4, dy, dp, carry=None):
    tt = o0.shape[0]
    s_len = tt - L

    def body(o0_ref, o1_ref, g_ref, w_ref, dy_ref, _dp_in, dp_ref, do_ref, dw_ref):
        i = pl.program_id(0)

        @pl.when(i == 0)
        def _():
            dw_ref[...] = jnp.zeros_like(dw_ref)
            dp_ref[...] = jnp.zeros_like(dp_ref)

        @pl.when(i >= 1)
        def _():
            o = o0_ref[...] + o1_ref[...]
            g = g_ref[...]
            w = w_ref[...]
            sg = _sig(g)
            dy_ = dy_ref[...]
            dsw = dy_ * (g * sg)
            outs, xhs = [], []
            for h in range(4):
                sl = slice(h * HGD, (h + 1) * HGD)
                oh = o[:, sl]
                r = lax.rsqrt(jnp.mean(oh * oh, axis=-1, keepdims=True) + EPS)
                xh = oh * r
                dxh = dsw[:, sl] * w[:, sl]
                outs.append(r * (dxh - xh * jnp.mean(dxh * xh, axis=-1, keepdims=True)))
                xhs.append(xh)
            xh = jnp.concatenate(xhs, axis=1)
            do_ref[...] = jnp.concatenate(outs, axis=1)
            dp_ref[...] = (dy_ * xh * w * (sg * (1.0 + g * (1.0 - sg)))).astype(BF16)
            dw_ref[...] += jnp.sum(dsw * xh, axis=0, keepdims=True)

    tok = pl.BlockSpec((TM, HGW), lambda i: (i, 0))
    lat = pl.BlockSpec((TM, HGW), lambda i: (jnp.maximum(i - 1, 0), 0))
    return _pcall(body, name="readout_bwd", grid=(tt // TM,),
                  in_specs=[tok, tok, pl.BlockSpec((TM, HGW), lambda i: (i, C_GHG)), _full((1, HGW)), lat, ANY],
                  out_specs=[pl.BlockSpec((TM, HGW), lambda i: (i, C_GHG)), lat, _full((1, HGW))],
                  out_shape=[jax.ShapeDtypeStruct(dp.shape, BF16), jax.ShapeDtypeStruct((s_len, HGW), F32),
                             jax.ShapeDtypeStruct((1, HGW), F32)],
                  aliases={5: 0}, carry=carry)(o0, o1, p, hw4, dy, dp)


def _rope_tables(s_len):
    t = np.arange(s_len)
    inv = ROPE_THETA ** (-np.arange(0, 32, 2, dtype=np.float64) / 32)
    def half(pos):
        ang = pos[:, None].astype(np.float64) * inv[None, :]
        return (np.concatenate([np.cos(ang), np.cos(ang)], 1), np.concatenate([-np.sin(ang), np.sin(ang)], 1))
    cr, sr = half(t // GRID_W)
    cc, sc = half(t % GRID_W)
    cos = np.concatenate([cr, cc, cr, cc], 1)
    sin = np.concatenate([sr, sc, sr, sc], 1)
    cos = np.concatenate([np.ones((L, 128)), cos], 0)
    sin = np.concatenate([np.zeros((L, 128)), sin], 0)
    return jnp.asarray(cos, F32), jnp.asarray(sin, F32)


def _blockdiag(n, w):
    i = np.arange(n)
    return jnp.asarray((i[:, None] // w == i[None, :] // w) / float(w), F32)


def _dup_matrix():
    m = np.zeros((128, 512), np.float32)
    for g in range(2):
        for j in range(4):
            for dd in range(HDIM):
                m[64 * g + dd, 256 * g + 64 * j + dd] = 1.0
    return m


def _rot(x):
    n = x.shape[1]
    lane = lax.broadcasted_iota(jnp.int32, x.shape, 1)
    return jnp.where((lane % 32) < 16, pltpu.roll(x, n - 16, 1), pltpu.roll(x, 16, 1))


def _qk_prep(p, cos, sin, qnw8, knw2, bd512, bd128, dup):
    tt = p.shape[0]

    def body(q_ref, kv_ref, cos_ref, sin_ref, qw_ref, kw_ref, b5_ref, b1_ref, dup_ref,
             qr_ref, k4_ref, v4_ref):
        cos_, sin_ = cos_ref[...], sin_ref[...]
        q = q_ref[...]
        qn = q * lax.rsqrt(_dot(q * q, b5_ref[...], prec=HI) + EPS) * qw_ref[...]
        cos4 = jnp.concatenate([cos_] * 4, axis=1)
        sin4 = jnp.concatenate([sin_] * 4, axis=1)
        qr_ref[...] = ((qn * cos4 + _rot(qn) * sin4) * (HDIM ** -0.5)).astype(BF16)
        kv = kv_ref[...]
        k, v = kv[:, :128], kv[:, 128:]
        kn = k * lax.rsqrt(_dot(k * k, b1_ref[...], prec=HI) + EPS) * kw_ref[...]
        kr = kn * cos_ + _rot(kn) * sin_
        k4_ref[...] = _bdot(kr, dup_ref[...]).astype(BF16)
        v4_ref[...] = _bdot(v, dup_ref[...]).astype(BF16)

    row = lambda w, cb: pl.BlockSpec((TM, w), lambda i: (i, cb))
    out = jax.ShapeDtypeStruct((tt, ATW), BF16)
    return _pcall(body, name="qk_prep", grid=(tt // TM,),
                  in_specs=[row(ATW, C_QRAW), row(256, C_KV), row(128, 0), row(128, 0),
                            _full((1, ATW)), _full((1, 128)), _full((ATW, ATW)), _full((128, 128)),
                            _full((128, ATW))],
                  out_specs=[row(ATW, 0)] * 3, out_shape=[out] * 3)(
                      p, p, cos, sin, qnw8, knw2, bd512, bd128, dup)


def _attn_masks(i, nb):
    r = lax.broadcasted_iota(jnp.int32, (BLK, 3 * BLK + L), 0)
    c = lax.broadcasted_iota(jnp.int32, (BLK, 3 * BLK + L), 1)
    kpos = (i - 1) * BLK + c
    loc = (jnp.abs(c - BLK - r) <= BLK) & (kpos >= 0) & (kpos < nb * BLK)
    return loc | (c >= 3 * BLK)


def _lane_mask(j):
    lane = lax.broadcasted_iota(jnp.int32, (1, 256), 1)
    return (lane // HDIM) == j


def _attn_specs(nb):
    blk = lambda off: pl.BlockSpec((BLK, ATW), lambda i: (jnp.clip(i + off, 0, nb - 1) + 2, 0))
    ctx = pl.BlockSpec((L, ATW), lambda i: (0, 0))
    return blk, ctx


def _attn_fwd(qr, k4, v4, sinks, carry=None):
    tt = qr.shape[0]
    s_len = tt - L
    nb = s_len // BLK

    def body(sk_ref, q_ref, kp, ko, kn, kc, vp, vo, vn, vc, y_ref, lse_ref):
        i = pl.program_id(0)
        valid = _attn_masks(i, nb)
        q = q_ref[...]
        ys, lses = [], []
        for g in range(2):
            gs = slice(256 * g, 256 * g + 256)
            kcat = jnp.concatenate([kp[:, gs], ko[:, gs], kn[:, gs], kc[:, gs]], axis=0)
            vcat = jnp.concatenate([vp[:, gs], vo[:, gs], vn[:, gs], vc[:, gs]], axis=0)
            qg = q[:, gs]
            og = jnp.zeros((BLK, 256), F32)
            lg = jnp.zeros((BLK, 256), F32)
            for j in range(4):
                lm = _lane_mask(j)
                sink = sk_ref[4 * g + j]
                s = jnp.where(valid, _dot(jnp.where(lm, qg, jnp.zeros_like(qg)), kcat, NT), -1e30)
                m = jnp.maximum(jnp.max(s, axis=-1, keepdims=True), sink)
                e = jnp.exp(s - m)
                den = jnp.sum(e, axis=-1, keepdims=True) + jnp.exp(sink - m)
                pr = e / den
                og = og + jnp.where(lm, _bdot(pr, vcat), 0.0)
                lg = lg + jnp.where(lm, m + jnp.log(den), 0.0)
            ys.append(og)
            lses.append(lg)
        y_ref[...] = jnp.concatenate(ys, axis=1).astype(BF16)
        lse_ref[...] = jnp.concatenate(lses, axis=1)

    blk, ctx = _attn_specs(nb)
    out = pl.BlockSpec((BLK, ATW), lambda i: (i, 0))
    return _pcall(body, name="attn_fwd", grid=(nb,),
                  in_specs=[pl.BlockSpec(memory_space=pltpu.SMEM), blk(0),
                            blk(-1), blk(0), blk(1), ctx, blk(-1), blk(0), blk(1), ctx],
                  out_specs=[out, out],
                  out_shape=[jax.ShapeDtypeStruct((s_len, ATW), BF16),
                             jax.ShapeDtypeStruct((s_len, ATW), F32)], carry=carry)(
                      sinks, qr, k4, k4, k4, k4, v4, v4, v4, v4)


def _attn_bwd(qr, k4, v4, sinks, y, lse, dy, carry=None):
    tt = qr.shape[0]
    s_len = tt - L
    nb = s_len // BLK

    def body(sk_ref, q_ref, kp, ko, kn, kc, vp, vo, vn, vc, y_ref, lse_ref, dy_ref,
             dq_ref, dkw_ref, dvw_ref, dkc_ref, dvc_ref, dsk_ref):
        i = pl.program_id(0)

        @pl.when(i == 0)
        def _():
            dkc_ref[...] = jnp.zeros_like(dkc_ref)
            dvc_ref[...] = jnp.zeros_like(dvc_ref)
            dsk_ref[...] = jnp.zeros_like(dsk_ref)

        valid = _attn_masks(i, nb)
        q = q_ref[...]
        dy_ = dy_ref[...]
        dly = dy_ * y_ref[...].astype(F32)
        lse_ = lse_ref[...]
        dqs = []
        for g in range(2):
            gs = slice(256 * g, 256 * g + 256)
            kcat = jnp.concatenate([kp[:, gs], ko[:, gs], kn[:, gs], kc[:, gs]], axis=0)
            vcat = jnp.concatenate([vp[:, gs], vo[:, gs], vn[:, gs], vc[:, gs]], axis=0)
            qg, dyg, dlg, lsg = q[:, gs], dy_[:, gs], dly[:, gs], lse_[:, gs]
            dqg = jnp.zeros((BLK, 256), F32)
            dkg = jnp.zeros((3 * BLK + L, 256), F32)
            dvg = jnp.zeros((3 * BLK + L, 256), F32)
            for j in range(4):
                lm = _lane_mask(j)
                sink = sk_ref[4 * g + j]
                qm = jnp.where(lm, qg, jnp.zeros_like(qg))
                dym = jnp.where(lm, dyg, 0.0).astype(BF16)
                lse_h = jnp.max(jnp.where(lm, lsg, -1e30), axis=-1, keepdims=True)
                delta = jnp.sum(jnp.where(lm, dlg, 0.0), axis=-1, keepdims=True)
                s = _dot(qm, kcat, NT)
                pr = jnp.where(valid, jnp.exp(s - lse_h), 0.0)
                dpr = _dot(dym, vcat, NT)
                dsc = pr * (dpr - delta)
                psink = jnp.exp(sink - lse_h)
                dsk_ref[4 * g + j:4 * g + j + 1, :] += jnp.broadcast_to(
                    -jnp.sum(psink * delta, axis=0, keepdims=True), (1, 128))
                dsb = dsc.astype(BF16)
                dqg = dqg + jnp.where(lm, _dot(dsb, kcat), 0.0)
                dkg = dkg + _dot(dsb, qm, TN)
                dvg = dvg + _dot(pr.astype(BF16), dym, TN)
            dqs.append(dqg)
            dkw_ref[0, :, gs] = dkg[:3 * BLK]
            dvw_ref[0, :, gs] = dvg[:3 * BLK]
            dkc_ref[:, gs] += dkg[3 * BLK:]
            dvc_ref[:, gs] += dvg[3 * BLK:]
        dq_ref[...] = jnp.concatenate(dqs, axis=1)

    blk, ctx = _attn_specs(nb)
    out = pl.BlockSpec((BLK, ATW), lambda i: (i, 0))
    win = pl.BlockSpec((1, 3 * BLK, ATW), lambda i: (i, 0, 0))
    acc = _full((L, ATW))
    return _pcall(body, name="attn_bwd", grid=(nb,),
                  in_specs=[pl.BlockSpec(memory_space=pltpu.SMEM), blk(0),
                            blk(-1), blk(0), blk(1), ctx, blk(-1), blk(0), blk(1), ctx, out, out, out],
                  out_specs=[out, win, win, acc, acc, _full((8, 128))],
                  out_shape=[jax.ShapeDtypeStruct((s_len, ATW), F32),
                             jax.ShapeDtypeStruct((nb, 3 * BLK, ATW), F32),
                             jax.ShapeDtypeStruct((nb, 3 * BLK, ATW), F32),
                             jax.ShapeDtypeStruct((L, ATW), F32), jax.ShapeDtypeStruct((L, ATW), F32),
                             jax.ShapeDtypeStruct((8, 128), F32)], carry=carry)(
                      sinks, qr, k4, k4, k4, k4, v4, v4, v4, v4, y, lse, dy)


def _attn_post(p, cos, sin, qnw8, knw2, bd512, bd128, dupt, dq, dkw, dvw, dkc, dvc, dp, carry=None):
    tt = p.shape[0]
    s_len = tt - L
    nb = s_len // BLK
    nctx = L // BLK

    def body(q_ref, kv_ref, cos_ref, sin_ref, qw_ref, kw_ref, b5_ref, b1_ref, dupt_ref,
             dq_ref, kwp, kwo, kwn, vwp, vwo, vwn, dkc_ref, dvc_ref, _dp_in,
             dp_ref, dqw_ref, dkw_ref):
        t = pl.program_id(0)
        j = t - nctx

        @pl.when(t == 0)
        def _():
            dqw_ref[...] = jnp.zeros_like(dqw_ref)
            dkw_ref[...] = jnp.zeros_like(dkw_ref)

        is_lat = t >= nctx
        cos_, sin_ = cos_ref[...], sin_ref[...]
        has_p = is_lat & (j >= 1)
        has_n = is_lat & (j <= nb - 2)
        dk4 = (jnp.where(is_lat, kwo[0], dkc_ref[...]) + jnp.where(has_p, kwp[0], 0.0)
               + jnp.where(has_n, kwn[0], 0.0))
        dv4 = (jnp.where(is_lat, vwo[0], dvc_ref[...]) + jnp.where(has_p, vwp[0], 0.0)
               + jnp.where(has_n, vwn[0], 0.0))
        dkr = _dot(dk4, dupt_ref[...], prec=HI)
        dv = _dot(dv4, dupt_ref[...], prec=HI)
        kv = kv_ref[...]
        k = kv[:, :128]
        kw = kw_ref[...]
        rk = lax.rsqrt(_dot(k * k, b1_ref[...], prec=HI) + EPS)
        xk = k * rk
        dkn = dkr * cos_ + _rot(dkr * sin_)
        dxk = dkn * kw
        dk = rk * (dxk - xk * _dot(dxk * xk, b1_ref[...], prec=HI))
        dkw_ref[...] += jnp.sum(dkn * xk, axis=0, keepdims=True)
        q = q_ref[...]
        qw = qw_ref[...]
        rq = lax.rsqrt(_dot(q * q, b5_ref[...], prec=HI) + EPS)
        xq = q * rq
        cos4 = jnp.concatenate([cos_] * 4, axis=1)
        sin4 = jnp.concatenate([sin_] * 4, axis=1)
        dqr = jnp.where(is_lat, dq_ref[...], 0.0) * (HDIM ** -0.5)
        dqn = dqr * cos4 + _rot(dqr * sin4)
        dxq = dqn * qw
        dqraw = rq * (dxq - xq * _dot(dxq * xq, b5_ref[...], prec=HI))
        dqw_ref[...] += jnp.sum(dqn * xq, axis=0, keepdims=True)
        dp_ref[...] = jnp.concatenate([dqraw, dk, dv], axis=1).astype(BF16)

    row = lambda w, cb: pl.BlockSpec((BLK, w), lambda t: (t, cb))
    lat = pl.BlockSpec((BLK, ATW), lambda t: (jnp.maximum(t - nctx, 0), 0))

    def part(off):
        return pl.BlockSpec((1, BLK, ATW), lambda t: (jnp.clip(t - nctx + off, 0, nb - 1), 1 - off, 0))

    cacc = pl.BlockSpec((BLK, ATW), lambda t: (jnp.minimum(t, nctx - 1), 0))
    return _pcall(body, name="attn_post", grid=(tt // BLK,),
                  in_specs=[row(ATW, C_QRAW), row(256, C_KV), row(128, 0), row(128, 0),
                            _full((1, ATW)), _full((1, 128)), _full((ATW, ATW)), _full((128, 128)),
                            _full((ATW, 128)), lat, part(-1), part(0), part(1), part(-1), part(0), part(1),
                            cacc, cacc, ANY],
                  out_specs=[pl.BlockSpec((BLK, 768), lambda t: (t, C_QKV)), _full((1, ATW)), _full((1, 128))],
                  out_shape=[jax.ShapeDtypeStruct(dp.shape, BF16), jax.ShapeDtypeStruct((1, ATW), F32),
                             jax.ShapeDtypeStruct((1, 128), F32)],
                  aliases={18: 0}, carry=carry)(p, p, cos, sin, qnw8, knw2, bd512, bd128, dupt,
                                   dq, dkw, dkw, dkw, dvw, dvw, dvw, dkc, dvc, dp)


def _merge(ah, aa, p):
    s_len = ah.shape[0]

    def body(ah_ref, aa_ref, gh_ref, ga_ref, m_ref):
        m_ref[...] = (_sig(gh_ref[...]) * ah_ref[...] + _sig(ga_ref[...]) * aa_ref[...]).astype(BF16)

    row = pl.BlockSpec((TM, D), lambda i: (i, 0))
    return _pcall(body, name="merge", grid=(s_len // TM,),
                  in_specs=[row, row, pl.BlockSpec((TM, D), lambda i: (i + 1, 2)),
                            pl.BlockSpec((TM, D), lambda i: (i + 1, 3))],
                  out_specs=row, out_shape=jax.ShapeDtypeStruct((s_len, D), BF16))(ah, aa, p, p)


def _merge_bwd(dm, ah, aa, p, carry=None):
    tt = p.shape[0]
    s_len = tt - L

    def body(dm_ref, ah_ref, aa_ref, gh_ref, ga_ref, dp_ref, dmh_ref, dma_ref):
        i = pl.program_id(0)

        @pl.when(i == 0)
        def _():
            dp_ref[...] = jnp.zeros_like(dp_ref)

        @pl.when(i >= 1)
        def _():
            dm_ = dm_ref[...]
            sh, sa = _sig(gh_ref[...]), _sig(ga_ref[...])
            dp_ref[...] = jnp.concatenate([dm_ * ah_ref[...] * sh * (1.0 - sh),
                                           dm_ * aa_ref[...] * sa * (1.0 - sa)], axis=1).astype(BF16)
            dmh_ref[...] = (dm_ * sh).astype(BF16)
            dma_ref[...] = (dm_ * sa).astype(BF16)

    lat = pl.BlockSpec((TM, D), lambda i: (jnp.maximum(i - 1, 0), 0))
    return _pcall(body, name="merge_bwd", grid=(tt // TM,),
                  in_specs=[lat, lat, lat, pl.BlockSpec((TM, D), lambda i: (i, 2)),
                            pl.BlockSpec((TM, D), lambda i: (i, 3))],
                  out_specs=[pl.BlockSpec((TM, 2 * D), lambda i: (i, C_GATES)), lat, lat],
                  out_shape=[jax.ShapeDtypeStruct((tt, NCOL), BF16), jax.ShapeDtypeStruct((s_len, D), BF16),
                             jax.ShapeDtypeStruct((s_len, D), BF16)], carry=carry)(dm, ah, aa, p, p)


def _local_step(x, ctx, tgt, mod, modc, nw1, nw2, lg, hw, qnw, knw, sinks,
                w_in, wts, dist=None):
    s_len = x.shape[0]
    tt = s_len + L
    tok = jnp.concatenate([ctx, x], axis=0)
    ss1 = jnp.stack([modc, mod[0:2]])
    ss2 = mod[3:5][None]
    g1, g2 = mod[2:3], mod[5:6]
    hw4 = jnp.tile(hw, (1, 4))
    qnw8 = jnp.tile(qnw, (1, 8))
    knw2 = jnp.tile(knw, (1, 2))
    cos, sin = _rope_tables(s_len)
    bd512, bd128 = _blockdiag(ATW, HDIM), _blockdiag(128, HDIM)
    dupm = _dup_matrix()
    dup, dupt = jnp.asarray(dupm, BF16), jnp.asarray(dupm.T, F32)
    tmt = tt // 2 if tt % 512 else 512

    def four(b):
        return b.reshape(4, 2 * b.shape[1], b.shape[2])

    def halves(g):
        return g.reshape(4, 2, g.shape[1] // 2, g.shape[2])

    h = _modulate(tok, nw1, ss1, name="mod1", sel=lambda i: jnp.minimum(i, 1))
    if dist is None:
        bh4, ba4, w_o, g4, u4, dn4 = wts
        p = _mm_in(h, w_in, tmt)
        o0, st0 = _hgrn_fwd(p, lg, rev=False)
        o1, st1 = _hgrn_fwd(p, lg, rev=True)
    else:
        core, chip = dist
        p, first = _mm_in(h, w_in, tmt, carry=_carry_gather(list(wts[0:3])))
        (o0, st0), (g8,) = _hgrn_fwd(p, lg, rev=False, carry=_carry_gather([wts[3]]))
        (o1, st1), (u8,) = _hgrn_fwd(p, lg, rev=True, carry=_carry_gather([wts[4]]))
        bh4, ba4, w_o, g4, u4 = four(first[0]), four(first[1]), four(first[2]).reshape(D, D), four(g8), four(u8)
    y_hg = _readout(o0, o1, p, hw4)
    qr, k4, v4 = _qk_prep(p, cos, sin, qnw8, knw2, bd512, bd128, dup)
    if dist is None:
        y_at, lse = _attn_fwd(qr, k4, v4, sinks)
    else:
        (y_at, lse), (dn8,) = _attn_fwd(qr, k4, v4, sinks, carry=_carry_gather([wts[5]]))
        dn4 = four(dn8)
    ah = _mm_cs(y_hg, bh4, name="mm_bh")
    aa = _mm_cs(y_at, ba4, name="mm_ba")
    mixed = _merge(ah, aa, p)
    ao = _mm(mixed, w_o, name="mm_o", tm=512, tn=D, tk=D)
    x1, h2 = _res1_mod2(x, ao, g1, nw2, ss2)
    a4, b4, z4 = _ffn_up(h2, g4, u4)
    y = _ffn_down(z4, dn4)
    sq, dx2, dyb, dg2 = _loss_head(x1, y, g2, tgt)

    da4, db4 = _ffn_dz(dyb, dn4, a4, b4)
    g_dn = _ffn_gdn(z4, dyb)
    dh2 = _ffn_dh2(da4, db4, g4, u4)
    g_g, g_u = _ffn_ggu(h2, da4, db4)
    dx1, dattn, dss2, dnw2, dg1 = _mod2_bwd(x1, dh2, dx2, ao, nw2, ss2, g1)
    dm = _mm(dattn, w_o, name="mm_dm", mode="nt", tm=512, tn=D, tk=D)
    g_o = _mm(mixed, dattn, name="mm_go", mode="tn", tm=D, tn=D, tk=512)
    if dist is None:
        dp, dmh, dma = _merge_bwd(dm, ah, aa, p)
    else:
        ffn_units = [halves(g_dn), halves(g_g), halves(g_u)]
        (dp, dmh, dma), ffn_recv = _merge_bwd(dm, ah, aa, p, carry=_carry_pairx(ffn_units))
        ffn_pairs = _rs_pair_add(ffn_units, ffn_recv, core)
    dy_hg = _mm_cs_nt(dmh, bh4, name="mm_dyh")
    dy_at = _mm_cs_nt(dma, ba4, name="mm_dya")
    g_bh = _mm_cs_tn(y_hg, dmh, D // 4, name="mm_gbh")
    g_ba = _mm_cs_tn(y_at, dma, D // 4, name="mm_gba")
    if dist is None:
        dp, do, dhw4 = _readout_bwd(o0, o1, p, hw4, dy_hg, dp)
        dq, dkw, dvw, dkc, dvc, dsk = _attn_bwd(qr, k4, v4, sinks, y_at, lse, dy_at)
        dp, dqnw8, dknw2 = _attn_post(p, cos, sin, qnw8, knw2, bd512, bd128, dupt, dq, dkw, dvw, dkc, dvc, dp)
    else:
        mix_units = [halves(g_bh), halves(g_ba), halves(g_o.reshape(4, D // 4, D))]
        (dp, do, dhw4), mix_recv = _readout_bwd(o0, o1, p, hw4, dy_hg, dp, carry=_carry_pairx(mix_units))
        mix_pairs = _rs_pair_add(mix_units, mix_recv, core)
        (dq, dkw, dvw, dkc, dvc, dsk), mix_contribs = _attn_bwd(qr, k4, v4, sinks, y_at, lse, dy_at,
                                                                carry=_carry_chipx(mix_pairs))
        mix_reds = _rs_chip_add(mix_pairs, mix_contribs, core, chip)
        (dp, dqnw8, dknw2), mix_done = _attn_post(p, cos, sin, qnw8, knw2, bd512, bd128, dupt, dq, dkw, dvw,
                                                  dkc, dvc, dp, carry=_carry_sibx(mix_reds))
    if dist is None:
        dp, dv0, dq0, dlg0 = _hgrn_bwd(p, lg, do, st0, dp, None, rev=False)
        dp, dlg1 = _hgrn_bwd(p, lg, do, st1, dp, (dv0, dq0), rev=True)
    else:
        (dp, dv0, dq0, dlg0), ffn_contribs = _hgrn_bwd(p, lg, do, st0, dp, None, rev=False,
                                                       carry=_carry_chipx(ffn_pairs))
        ffn_reds = _rs_chip_add(ffn_pairs, ffn_contribs, core, chip)
        (dp, dlg1), ffn_done = _hgrn_bwd(p, lg, do, st1, dp, (dv0, dq0), rev=True, carry=_carry_sibx(ffn_reds))
    dh = _mm_dh(dp, w_in, tmt)
    g_in = _mm_gin(dp, h, tmt)
    if dist is None:
        gx, dss1, dnw1 = _mod1_bwd(tok, dh, dx1, nw1, ss1)
        rs = None
    else:
        in_units = [halves(g_in.reshape(4, NCOL // 4, D))]
        (gx, dss1, dnw1), in_recv = _mod1_bwd(tok, dh, dx1, nw1, ss1, carry=_carry_pairx(in_units))
        rs = dict(ffn_done=ffn_done, mix_done=mix_done, in_units=in_units, in_recv=in_recv)

    dmod = jnp.concatenate([dss1[1], dg1, dss2, dg2], axis=0)
    dmodc = dss1[0]
    raw = (dss1, dg1, dss2, dg2, dnw1, dnw2, dhw4, dqnw8, dknw2, dsk, dlg0, dlg1)
    small = dict(raw=raw, dmod=dmod, dmodc=dmodc, dnw1=dnw1, dnw2=dnw2,
                 dhw=dhw4.reshape(4, HGD).sum(0, keepdims=True),
                 dqnw=dqnw8.reshape(8, HDIM).sum(0, keepdims=True),
                 dknw=dknw2.reshape(2, HDIM).sum(0, keepdims=True),
                 dsinks=dsk[:, 0], dlg=jnp.concatenate([dlg0, dlg1], axis=0))
    big = dict(w_in=g_in, w_bh=g_bh, w_ba=g_ba, w_o=g_o, w_g=g_g, w_u=g_u, w_dn=g_dn)
    return sq, gx, big, small, rs


def _place():
    x, y, c = lax.axis_index("x"), lax.axis_index("y"), lax.axis_index("c")
    return x, y, c


def _gather_blocks(x_refs, out_refs, send_sems, recv_sems, local_sems):
    n = len(out_refs)
    x, y, c = _place()
    me, sibling = (x, y, c), (x, y, 1 - c)
    chips = [(1 - x, y), (x, 1 - y), (1 - x, 1 - y)]

    def slot(u, px, py, pc):
        return out_refs[u].at[4 * px + 2 * py + pc]

    def copy(u, k, block, to, src=None):
        return pltpu.make_async_remote_copy(
            src_ref=slot(u, *block) if src is None else src, dst_ref=slot(u, *block),
            send_sem=send_sems.at[u, k], recv_sem=recv_sems.at[u, k], device_id=to, device_id_type=MESH)

    mines = []
    if x_refs is not None:
        mines = [pltpu.make_async_copy(x_refs[u], slot(u, *me), local_sems.at[u]) for u in range(n)]
    for cp in mines:
        cp.start()
    first = []
    for u in range(n):
        src = None if x_refs is None else x_refs[u]
        first.append(copy(u, 0, me, sibling, src=src))
        first += [copy(u, 1 + j, me, (*chip, c), src=src) for j, chip in enumerate(chips)]
    for cp in first:
        cp.start()
    passed = []
    for j, chip in enumerate(chips):
        for u in range(n):
            copy(u, 1 + j, (*chip, c), me).wait_recv()
            fwd = copy(u, 4 + j, (*chip, c), sibling)
            fwd.start()
            passed.append(fwd)
    for u in range(n):
        copy(u, 0, sibling, me).wait_recv()
    for j, chip in enumerate(chips):
        for u in range(n):
            copy(u, 4 + j, (*chip, 1 - c), me).wait_recv()
    for cp in first + passed:
        cp.wait_send()
    for cp in mines:
        cp.wait()


def _gather_sems(n):
    return [pltpu.SemaphoreType.DMA((n, 7)), pltpu.SemaphoreType.DMA((n, 7)), pltpu.SemaphoreType.DMA((n,))]


def _allgather(blks, *, name, in_vmem):
    n = len(blks)
    space = pltpu.VMEM if in_vmem else pl.ANY

    def body(*refs):
        _gather_blocks(refs[:n], refs[n:2 * n], *refs[2 * n:])

    return pl.pallas_call(
        body, name=name, out_shape=[jax.ShapeDtypeStruct((8,) + b.shape, b.dtype) for b in blks],
        in_specs=[pl.BlockSpec(memory_space=space)] * n, out_specs=[pl.BlockSpec(memory_space=space)] * n,
        scratch_shapes=_gather_sems(n))(*blks)


def _cast_place(ws, c, dev):
    n = len(ws)

    def body(s_ref, *refs):
        for u in range(n):
            refs[n + u][0] = refs[u][...].astype(BF16)

    in_specs, out_specs, out_shape = [], [], []
    for w in ws:
        q, cols = w.shape[0] // 4, w.shape[1]
        in_specs.append(pl.BlockSpec((q, cols), lambda i, s: (2 * s[0] + i, 0)))
        out_specs.append(pl.BlockSpec((1, q, cols), lambda i, s: (s[1], i, 0)))
        out_shape.append(jax.ShapeDtypeStruct((8, 2 * q, cols), BF16))
    return pl.pallas_call(
        body, name="cast_place",
        grid_spec=pltpu.PrefetchScalarGridSpec(num_scalar_prefetch=1, grid=(2,), in_specs=in_specs,
                                               out_specs=out_specs),
        out_shape=out_shape,
        compiler_params=pltpu.CompilerParams(vmem_limit_bytes=48 << 20))(jnp.stack([c, dev]), *ws)


def _gather_phases(out_refs, send_sems, recv_sems):
    n = len(out_refs)
    x, y, c = _place()
    me, sibling = (x, y, c), (x, y, 1 - c)
    chips = [(1 - x, y), (x, 1 - y), (1 - x, 1 - y)]

    def copy(u, k, block, to):
        px, py, pc = block
        ref = out_refs[u].at[4 * px + 2 * py + pc]
        return pltpu.make_async_remote_copy(src_ref=ref, dst_ref=ref, send_sem=send_sems.at[u, k],
                                            recv_sem=recv_sems.at[u, k], device_id=to, device_id_type=MESH)

    def start():
        for u in range(n):
            copy(u, 0, me, sibling).start()
            for j, chip in enumerate(chips):
                copy(u, 1 + j, me, (*chip, c)).start()

    def mid():
        for j, chip in enumerate(chips):
            for u in range(n):
                copy(u, 1 + j, (*chip, c), me).wait_recv()
                copy(u, 4 + j, (*chip, c), sibling).start()

    def end():
        for u in range(n):
            copy(u, 0, sibling, me).wait_recv()
        for j, chip in enumerate(chips):
            for u in range(n):
                copy(u, 4 + j, (*chip, 1 - c), me).wait_recv()
        for u in range(n):
            copy(u, 0, me, sibling).wait_send()
            for j, chip in enumerate(chips):
                copy(u, 1 + j, me, (*chip, c)).wait_send()
                copy(u, 4 + j, (*chip, c), sibling).wait_send()

    return start, mid, end


def _carry_gather(bufs):
    n = len(bufs)
    return _Carry(bufs, [jax.ShapeDtypeStruct(b.shape, b.dtype) for b in bufs], {u: u for u in range(n)},
                  [pltpu.SemaphoreType.DMA((n, 7)), pltpu.SemaphoreType.DMA((n, 7))],
                  lambda ins, outs, sems: _gather_phases(outs, *sems))


def _allgather_inplace(bufs, *, name):
    n = len(bufs)

    def body(*refs):
        for phase in _gather_phases(refs[n:2 * n], *refs[2 * n:]):
            phase()

    return pl.pallas_call(
        body, name=name, out_shape=[jax.ShapeDtypeStruct(b.shape, b.dtype) for b in bufs],
        in_specs=[ANY] * n, out_specs=[ANY] * n, input_output_aliases={u: u for u in range(n)},
        scratch_shapes=[pltpu.SemaphoreType.DMA((n, 7)), pltpu.SemaphoreType.DMA((n, 7))])(*bufs)


def _ag_small(raw):
    def body(dss1, dg1, dss2, dg2, dnw1, dnw2, dhw4, dqnw8, dknw2, dsk, dlg0, dlg1,
             out_ref, tot_ref, blk, send_sems, recv_sems, local_sems):
        blk[...] = jnp.zeros_like(blk)
        blk[0:2, :] = dss1[1]
        blk[2:3, :] = dg1[...]
        blk[3:5, :] = dss2[...]
        blk[5:6, :] = dg2[...]
        blk[6:8, :] = dss1[0]
        blk[8:9, :] = dnw1[...]
        blk[9:10, :] = dnw2[...]
        blk[10:11, 0:HGW] = dhw4[...]
        blk[10:11, HGW:D] = dqnw8[...]
        blk[11:12, 0:128] = dknw2[...]
        blk[12:14, 0:HGW] = dlg0[0]
        blk[14:16, 0:HGW] = dlg1[0]
        blk[16:24, 0:128] = dsk[...]
        _gather_blocks([blk], [out_ref], send_sems, recv_sems, local_sems)
        acc = out_ref[0]
        for i in range(1, 8):
            acc = acc + out_ref[i]
        tot_ref[...] = acc

    vm = pl.BlockSpec(memory_space=pltpu.VMEM)
    return pl.pallas_call(
        body, name="ag_small",
        out_shape=[jax.ShapeDtypeStruct((8, 24, D), F32), jax.ShapeDtypeStruct((24, D), F32)],
        in_specs=[vm] * 12, out_specs=[vm, vm],
        scratch_shapes=[pltpu.VMEM((24, D), F32)] + _gather_sems(1))(*raw)


def _rs_pair_exchange(units):
    n = len(units)

    def body(*refs):
        start, _, end = _pairx_phases(refs[:n], refs[n:2 * n], *refs[2 * n:])
        start()
        end()

    return pl.pallas_call(
        body, name="rs_pair_exchange", out_shape=_pairx_shapes(units),
        in_specs=[ANY] * n, out_specs=[ANY] * n,
        scratch_shapes=[pltpu.SemaphoreType.DMA((n, 4)), pltpu.SemaphoreType.DMA((n, 4))])(*units)


def _pairx_shapes(units):
    return [jax.ShapeDtypeStruct((4,) + g.shape[2:], g.dtype) for g in units]


def _pairx_phases(g_refs, r_refs, send_sems, recv_sems):
    n = len(g_refs)
    x, y, c = _place()
    cps = [pltpu.make_async_remote_copy(
        src_ref=g_refs[u].at[j, 1 - c], dst_ref=r_refs[u].at[j], send_sem=send_sems.at[u, j],
        recv_sem=recv_sems.at[u, j], device_id=(x, y, 1 - c), device_id_type=MESH)
        for u in range(n) for j in range(4)]

    def start():
        for cp in cps:
            cp.start()

    def end():
        for cp in cps:
            cp.wait()

    return start, None, end


def _carry_pairx(units):
    n = len(units)
    return _Carry(units, _pairx_shapes(units), {},
                  [pltpu.SemaphoreType.DMA((n, 4)), pltpu.SemaphoreType.DMA((n, 4))],
                  lambda ins, outs, sems: _pairx_phases(ins, outs, *sems))


def _rs_pair_add(units, recvs, c):
    n = len(units)

    def body(c_ref, *refs):
        for u in range(n):
            refs[2 * n + u][...] = (refs[u][0] + refs[n + u][...]).astype(BF16)

    in_specs, out_specs, out_shape = [], [], []
    for g in units:
        h, w = g.shape[2] // 2, g.shape[3]
        in_specs.append(pl.BlockSpec((1, 1, h, w), lambda j, i, cr: (j, cr[0], i, 0)))
    for g in units:
        h, w = g.shape[2] // 2, g.shape[3]
        in_specs.append(pl.BlockSpec((1, h, w), lambda j, i, cr: (j, i, 0)))
        out_specs.append(pl.BlockSpec((1, h, w), lambda j, i, cr: (j, i, 0)))
        out_shape.append(jax.ShapeDtypeStruct((4, 2 * h, w), BF16))
    return pl.pallas_call(
        body, name="rs_pair_add",
        grid_spec=pltpu.PrefetchScalarGridSpec(num_scalar_prefetch=1, grid=(4, 2), in_specs=in_specs,
                                               out_specs=out_specs),
        out_shape=out_shape,
        compiler_params=pltpu.CompilerParams(vmem_limit_bytes=48 << 20))(c.reshape(1), *units, *recvs)


def _rs_chip_exchange(pairs):
    n = len(pairs)

    def body(*refs):
        start, _, end = _chipx_phases(refs[:n], refs[n:2 * n], *refs[2 * n:])
        start()
        end()

    return pl.pallas_call(
        body, name="rs_chip_exchange", out_shape=[jax.ShapeDtypeStruct(p.shape, p.dtype) for p in pairs],
        in_specs=[ANY] * n, out_specs=[ANY] * n,
        scratch_shapes=[pltpu.SemaphoreType.DMA((n, 3)), pltpu.SemaphoreType.DMA((n, 3))])(*pairs)


def _chipx_phases(p_refs, r_refs, send_sems, recv_sems):
    n = len(p_refs)
    x, y, c = _place()
    k = 2 * x + y
    sends = []
    for d in range(1, 4):
        j = (k + d) % 4
        for u in range(n):
            sends.append(pltpu.make_async_remote_copy(
                src_ref=p_refs[u].at[j], dst_ref=r_refs[u].at[k], send_sem=send_sems.at[u, d - 1],
                recv_sem=recv_sems.at[u, d - 1], device_id=(j // 2, j % 2, c), device_id_type=MESH))

    def start():
        for cp in sends:
            cp.start()

    def end():
        for d in range(1, 4):
            src = (k + 4 - d) % 4
            for u in range(n):
                pltpu.make_async_remote_copy(
                    src_ref=p_refs[u].at[src], dst_ref=r_refs[u].at[src], send_sem=send_sems.at[u, d - 1],
                    recv_sem=recv_sems.at[u, d - 1], device_id=(x, y, c), device_id_type=MESH).wait_recv()
        for cp in sends:
            cp.wait_send()

    return start, None, end


def _carry_chipx(pairs):
    n = len(pairs)
    return _Carry(pairs, [jax.ShapeDtypeStruct(p.shape, p.dtype) for p in pairs], {},
                  [pltpu.SemaphoreType.DMA((n, 3)), pltpu.SemaphoreType.DMA((n, 3))],
                  lambda ins, outs, sems: _chipx_phases(ins, outs, *sems))


def _rs_chip_add(pairs, contribs, c, chip):
    n = len(pairs)

    def body(s_ref, *refs):
        for u in range(n):
            a, b, c_, d = refs[4 * u:4 * u + 4]
            refs[4 * n + u][0] = ((a[0].astype(F32) + b[0].astype(F32)) + c_[0].astype(F32)) + d[0].astype(F32)

    in_specs, out_specs, out_shape, args = [], [], [], []
    for p, r in zip(pairs, contribs):
        h, w = p.shape[1] // 2, p.shape[2]
        in_specs += [pl.BlockSpec((1, h, w), functools.partial(lambda d, i, s: ((s[1] + d) % 4, i, 0), d))
                     for d in range(4)]
        args += [p, r, r, r]
        out_specs.append(pl.BlockSpec((1, h, w), lambda i, s: (s[0], i, 0)))
        out_shape.append(jax.ShapeDtypeStruct((2, 2 * h, w), F32))
    return pl.pallas_call(
        body, name="rs_chip_add",
        grid_spec=pltpu.PrefetchScalarGridSpec(num_scalar_prefetch=1, grid=(2,), in_specs=in_specs,
                                               out_specs=out_specs),
        out_shape=out_shape,
        compiler_params=pltpu.CompilerParams(vmem_limit_bytes=48 << 20))(jnp.stack([c, chip]), *args)


def _rs_sibling_gather(reds):
    n = len(reds)

    def body(*refs):
        start, _, end = _sibx_phases(refs[n:2 * n], *refs[2 * n:])
        start()
        end()

    return pl.pallas_call(
        body, name="rs_sibling_gather", out_shape=[jax.ShapeDtypeStruct(r.shape, r.dtype) for r in reds],
        in_specs=[ANY] * n, out_specs=[ANY] * n, input_output_aliases={u: u for u in range(n)},
        scratch_shapes=[pltpu.SemaphoreType.DMA((n,))] * 2)(*reds)


def _sibx_phases(o_refs, send_sems, recv_sems):
    n = len(o_refs)
    x, y, c = _place()
    cps = [pltpu.make_async_remote_copy(
        src_ref=o_refs[u].at[c], dst_ref=o_refs[u].at[c], send_sem=send_sems.at[u], recv_sem=recv_sems.at[u],
        device_id=(x, y, 1 - c), device_id_type=MESH) for u in range(n)]

    def start():
        for cp in cps:
            cp.start()

    def end():
        for u in range(n):
            cps[u].wait_send()
            pltpu.make_async_remote_copy(
                src_ref=o_refs[u].at[1 - c], dst_ref=o_refs[u].at[1 - c], send_sem=send_sems.at[u],
                recv_sem=recv_sems.at[u], device_id=(x, y, 1 - c), device_id_type=MESH).wait_recv()

    return start, None, end


def _carry_sibx(reds):
    n = len(reds)
    return _Carry(reds, [jax.ShapeDtypeStruct(r.shape, r.dtype) for r in reds], {u: u for u in range(n)},
                  [pltpu.SemaphoreType.DMA((n,))] * 2, lambda ins, outs, sems: _sibx_phases(outs, *sems))


def _ada_fwd(c16, w, b):
    n = w.shape[1]
    tn = 512

    def body(c_ref, w_ref, b_ref, o_ref):
        cc = c_ref[...]
        o_ref[...] = _dot(cc * _sig(cc), w_ref[...], prec=HI) + b_ref[...]

    return _pcall(body, name="ada_fwd", grid=(n // tn,),
                  in_specs=[_full((16, D)), pl.BlockSpec((D, tn), lambda j: (0, j)),
                            pl.BlockSpec((1, tn), lambda j: (0, j))],
                  out_specs=pl.BlockSpec((16, tn), lambda j: (0, j)),
                  out_shape=jax.ShapeDtypeStruct((16, n), F32))(c16, w, b)


def _ada_bwd(c16, dmod16, w):
    n = w.shape[1]
    tn = 512

    def body(c_ref, d_ref, w_ref, gw_ref, gc_ref):
        j = pl.program_id(0)

        @pl.when(j == 0)
        def _():
            gc_ref[...] = jnp.zeros_like(gc_ref)

        cc = c_ref[...]
        dm = d_ref[...]
        gw_ref[...] = _dot(cc * _sig(cc), dm, TN, prec=HI)
        gc_ref[...] += _dot(dm, w_ref[...], NT, prec=HI)

    return _pcall(body, name="ada_bwd", grid=(n // tn,),
                  in_specs=[_full((16, D)), pl.BlockSpec((16, tn), lambda j: (0, j)),
                            pl.BlockSpec((D, tn), lambda j: (0, j))],
                  out_specs=[pl.BlockSpec((D, tn), lambda j: (0, j)), _full((16, D))],
                  out_shape=[jax.ShapeDtypeStruct((D, n), F32),
                             jax.ShapeDtypeStruct((16, D), F32)])(c16, dmod16, w)


def _adam_math(w, g, m, v):
    c1 = 1.0 - ADAM_B1 ** ADAM_STEP
    c2 = 1.0 - ADAM_B2 ** ADAM_STEP
    nm = ADAM_B1 * m + (1.0 - ADAM_B1) * g
    nv = ADAM_B2 * v + (1.0 - ADAM_B2) * (g * g)
    return -ADAM_LR * ((nm / c1) / (jnp.sqrt(nv / c2) + ADAM_EPS) + ADAM_WD * w), nm, nv


def _adamw_small(ws, gs, ms, vs):
    n = len(ws)

    def body(*refs):
        for u in range(n):
            d_, nm, nv = _adam_math(refs[u][...], refs[n + u][...], refs[2 * n + u][...], refs[3 * n + u][...])
            refs[4 * n + u][...] = d_
            refs[5 * n + u][...] = nm
            refs[6 * n + u][...] = nv

    specs = [_full(w.shape) for w in ws]
    shapes = [jax.ShapeDtypeStruct(w.shape, F32) for w in ws]
    out = _pcall(body, name="adamw_small", grid=(1,), in_specs=specs * 4, out_specs=specs * 3,
                 out_shape=shapes * 3)(*ws, *gs, *ms, *vs)
    return out[:n], out[n:2 * n], out[2 * n:]


def _cctx_grad(parts, c_ctx):
    def body(p_ref, c_ref, o_ref):
        acc = p_ref[0:1, :]
        for k in range(1, 4):
            acc = acc + p_ref[k:k + 1, :]
        cc = c_ref[...]
        s = _sig(cc)
        o_ref[...] = acc * (s * (1.0 + cc * (1.0 - s)))

    return _pcall(body, name="cctx_grad", grid=(1,), in_specs=[_full(parts.shape), _full((1, D))],
                  out_specs=_full((1, D)), out_shape=jax.ShapeDtypeStruct((1, D), F32))(parts, c_ctx)


ADAM_STEPS = 8


def _adamw_multi(ws, gs, ms, vs, *, name, carry=None):
    n = len(ws)

    def body(*refs):
        for u in range(n):
            refs[4 * n + u][...], refs[5 * n + u][...], refs[6 * n + u][...] = _adam_math(
                refs[u][...], refs[n + u][...], refs[2 * n + u][...], refs[3 * n + u][...])

    specs = [pl.BlockSpec((w.shape[0] // ADAM_STEPS, w.shape[1]), lambda i: (i, 0)) for w in ws]
    shapes = [jax.ShapeDtypeStruct(w.shape, F32) for w in ws]
    res = _pcall(body, name=name, grid=(ADAM_STEPS,), in_specs=specs * 4, out_specs=specs * 3,
                 out_shape=shapes * 3, carry=carry)(*ws, *gs, *ms, *vs)
    out, extra = res if carry is not None else (res, None)
    return (out[:n], out[n:2 * n], out[2 * n:]), extra


def kernel(x, c, ctx, c_ctx, w_ada, b_ada, norm_mix_w, norm_ffn_w, w_in, hgrn_lb_logits, hgrn_norm_w, q_norm_w, k_norm_w, attn_sinks, w_branch_hgrn, w_branch_attn, w_out, w_ffn_gate, w_ffn_up, w_ffn_down, loss_target, m_c_ctx, m_w_ada, m_b_ada, m_norm_mix_w, m_norm_ffn_w, m_w_in, m_hgrn_lb_logits, m_hgrn_norm_w, m_q_norm_w, m_k_norm_w, m_attn_sinks, m_w_branch_hgrn, m_w_branch_attn, m_w_out, m_w_ffn_gate, m_w_ffn_up, m_w_ffn_down, v_c_ctx, v_w_ada, v_b_ada, v_norm_mix_w, v_norm_ffn_w, v_w_in, v_hgrn_lb_logits, v_hgrn_norm_w, v_q_norm_w, v_k_norm_w, v_attn_sinks, v_w_branch_hgrn, v_w_branch_attn, v_w_out, v_w_ffn_gate, v_w_ffn_up, v_w_ffn_down):
    xi, yi, ci = _place()
    chip = 2 * xi + yi
    dev = 2 * chip + ci
    s_len = x.shape[1]

    lbrow = jnp.pad(hgrn_lb_logits.reshape(1, 512), ((0, 0), (0, D - 512)))
    blk = jnp.concatenate([c, lbrow, jnp.zeros((6, D), F32)], axis=0)
    g0, = _allgather([blk], name="ag_cond", in_vmem=True)
    c16 = jnp.concatenate([g0[:, 0], c_ctx[None], jnp.zeros((7, D), F32)], axis=0)
    lg = g0[0::2, 1, :512].reshape(4, 2, 2, 128).transpose(1, 2, 0, 3).reshape(2, 2, HGW)

    nada = w_ada.shape[2]
    b_sh = lax.dynamic_slice(b_ada, (0, chip * nada), (1, nada))
    mod_sh = _ada_fwd(c16, w_ada[0], b_sh)
    g1, = _allgather([mod_sh], name="ag_mod", in_vmem=True)
    modall = g1[0::2].transpose(1, 0, 2).reshape(16, 4 * nada)
    mod = lax.dynamic_slice(modall, (dev, 0), (1, 6 * D)).reshape(6, D)
    modc = modall[8].reshape(6, D)[:2]

    shards = [w_in[0].T, w_branch_hgrn[0], w_branch_attn[0], w_out[0], w_ffn_gate[0].T, w_ffn_up[0].T,
              w_ffn_down[0]]
    bufs = _cast_place(shards, ci, dev)
    in8, = _allgather_inplace(bufs[0:1], name="ag_w_in")

    sq, gx, _, small, rs = _local_step(
        x[0], ctx[0], loss_target[0], mod, modc, norm_mix_w, norm_ffn_w, lg, hgrn_norm_w, q_norm_w,
        k_norm_w, attn_sinks[0], in8.reshape(NCOL, D), bufs[1:], dist=(ci, chip))
    loss = lax.psum(0.5 * jnp.sum(sq) / D, ("x", "y", "c"))

    def whole(r):
        return r.reshape(2 * r.shape[1], r.shape[2])

    g_dn, g_g, g_u = [whole(r) for r in rs["ffn_done"]]
    g_bh, g_ba, g_o = [whole(r) for r in rs["mix_done"]]
    in_pairs = _rs_pair_add(rs["in_units"], rs["in_recv"], ci)

    g2, tot = _ag_small(small["raw"])
    dmodc_tot = jnp.pad(tot[6:8].reshape(1, 2 * D), ((0, 0), (0, 4 * D)))
    g_b_ada = tot[0:6].reshape(1, 6 * D) + dmodc_tot
    dmod16 = jnp.concatenate([g2[:, 0:6].reshape(8, 6 * D), dmodc_tot, jnp.zeros((7, 6 * D), F32)], axis=0)
    g_w_ada, gc_part = _ada_bwd(c16, lax.dynamic_slice(dmod16, (0, chip * nada), (16, nada)), w_ada[0])
    g3, = _allgather([gc_part[8:16]], name="ag_cctx", in_vmem=True)
    g_c_ctx = _cctx_grad(g3[0::2, 0], c_ctx[None])[0]
    g_nw1 = tot[8:9]
    g_nw2 = tot[9:10]
    g_hw = tot[10, :HGW].reshape(4, HGD).sum(0, keepdims=True)
    g_qnw = tot[10, HGW:].reshape(8, HDIM).sum(0, keepdims=True)
    g_knw = tot[11, :128].reshape(2, HDIM).sum(0, keepdims=True)
    g_sinks = tot[16:24, 0][None]
    g_lg = lax.dynamic_slice(tot[12:16, :HGW].reshape(2, 2, HGW), (0, 0, chip * 128), (2, 2, 128))

    names = ["c_ctx", "w_ada", "b_ada", "norm_mix_w", "norm_ffn_w", "w_in", "hgrn_lb_logits", "hgrn_norm_w",
             "q_norm_w", "k_norm_w", "attn_sinks", "w_branch_hgrn", "w_branch_attn", "w_out", "w_ffn_gate",
             "w_ffn_up", "w_ffn_down"]
    ws = dict(zip(names, [c_ctx, w_ada, b_ada, norm_mix_w, norm_ffn_w, w_in, hgrn_lb_logits, hgrn_norm_w,
                          q_norm_w, k_norm_w, attn_sinks, w_branch_hgrn, w_branch_attn, w_out, w_ffn_gate,
                          w_ffn_up, w_ffn_down]))
    ms = dict(zip(names, [m_c_ctx, m_w_ada, m_b_ada, m_norm_mix_w, m_norm_ffn_w, m_w_in, m_hgrn_lb_logits,
                          m_hgrn_norm_w, m_q_norm_w, m_k_norm_w, m_attn_sinks, m_w_branch_hgrn,
                          m_w_branch_attn, m_w_out, m_w_ffn_gate, m_w_ffn_up, m_w_ffn_down]))
    vs = dict(zip(names, [v_c_ctx, v_w_ada, v_b_ada, v_norm_mix_w, v_norm_ffn_w, v_w_in, v_hgrn_lb_logits,
                          v_hgrn_norm_w, v_q_norm_w, v_k_norm_w, v_attn_sinks, v_w_branch_hgrn,
                          v_w_branch_attn, v_w_out, v_w_ffn_gate, v_w_ffn_up, v_w_ffn_down]))
    transposed = ("w_in", "w_ffn_gate", "w_ffn_up")

    def view(a, n):
        return a[0].T if n in transposed else a[0]

    def unview(a, n):
        return a.T[None] if n in transposed else a[None]

    delta, new_m, new_v, grads = {}, {}, {}, {}

    def big_adamw(group, gs, name, carry=None):
        (d_, m_, v_), extra = _adamw_multi([view(ws[n], n) for n in group], gs, [view(ms[n], n) for n in group],
                                           [view(vs[n], n) for n in group], name=name, carry=carry)
        for i, n in enumerate(group):
            grads[n], delta[n], new_m[n], new_v[n] = (unview(gs[i], n), unview(d_[i], n), unview(m_[i], n),
                                                      unview(v_[i], n))
        return extra

    in_contribs = big_adamw(["w_ffn_down", "w_ffn_gate", "w_ffn_up", "w_out", "w_branch_hgrn", "w_branch_attn"],
                            [g_dn, g_g, g_u, g_o, g_bh, g_ba], "adamw_first", carry=_carry_chipx(in_pairs))
    in_reds = _rs_chip_add(in_pairs, in_contribs, ci, chip)
    g_in, = [whole(r) for r in _rs_sibling_gather(in_reds)]
    big_adamw(["w_in", "w_ada"], [g_in, g_w_ada], "adamw_second")
    grads.update(c_ctx=g_c_ctx, b_ada=g_b_ada, norm_mix_w=g_nw1, norm_ffn_w=g_nw2, hgrn_lb_logits=g_lg,
                 hgrn_norm_w=g_hw, q_norm_w=g_qnw, k_norm_w=g_knw, attn_sinks=g_sinks)
    small_names = [n for n in names if n not in delta]

    def two_d(a):
        return a.reshape(1, -1) if a.ndim == 1 else a

    sd, sm_, sv = _adamw_small(*[[two_d(d[n]) for n in small_names] for d in (ws, grads, ms, vs)])
    for i, n in enumerate(small_names):
        for dst, src in ((delta, sd), (new_m, sm_), (new_v, sv)):
            dst[n] = src[i].reshape(ws[n].shape)
    return (loss, gx[None], *[grads[n] for n in names], *[delta[n] for n in names],
            *[new_m[n] for n in names], *[new_v[n] for n in names])
```

```python
import functools

import numpy as np
import jax
import jax.numpy as jnp
from jax import lax
from jax.experimental import pallas as pl
from jax.experimental.pallas import tpu as pltpu

F32 = jnp.float32
BF16 = jnp.bfloat16
HI = lax.Precision.HIGHEST
MESH = pl.DeviceIdType.MESH

D = 1024
L = 256
TM = 256
HGW = 512
HGD = 128
CH = 32
ATW = 512
HDIM = 64
BLK = 128
GRID_W = 64
DFF = 2816
NCOL = 5376
EPS = 1e-6
ROPE_THETA = 10000.0

C_FB, C_INP, C_QHG, C_FF = 0, 1, 2, 3
C_GATES = 1
C_GHG, C_QRAW = 8, 9
C_KV = 20
C_QKV = 6

ADAM_LR, ADAM_B1, ADAM_B2, ADAM_EPS, ADAM_WD, ADAM_STEP = 0.001, 0.9, 0.999, 1e-08, 0.01, 10

NN = (((1,), (0,)), ((), ()))
NT = (((1,), (1,)), ((), ()))
TN = (((0,), (0,)), ((), ()))


def _dot(a, b, dims=NN, prec=None):
    return lax.dot_general(a, b, dims, precision=prec, preferred_element_type=F32)


def _bdot(a, b, dims=NN):
    return _dot(a.astype(BF16), b.astype(BF16), dims)


def _sig(x):
    return 1.0 / (1.0 + jnp.exp(-x))


class _Carry:
    def __init__(self, ins, outs, aliases, scratch, phases):
        self.ins, self.outs, self.aliases, self.scratch, self.phases = ins, outs, aliases, scratch, phases


def _carry_join(a, b):
    na_in, na_out, na_sc = len(a.ins), len(a.outs), len(a.scratch)
    aliases = dict(a.aliases)
    aliases.update({na_in + i: na_out + o for i, o in b.aliases.items()})

    def phases(ins, outs, sems):
        pa = a.phases(ins[:na_in], outs[:na_out], sems[:na_sc])
        pb = b.phases(ins[na_in:], outs[na_out:], sems[na_sc:])

        def both(fa, fb):
            if fa is None and fb is None:
                return None

            def run():
                for fn in (fa, fb):
                    if fn is not None:
                        fn()
            return run

        return tuple(both(fa, fb) for fa, fb in zip(pa, pb))

    return _Carry(list(a.ins) + list(b.ins), list(a.outs) + list(b.outs), aliases,
                  list(a.scratch) + list(b.scratch), phases)


def _pcall(body, *, name, grid, in_specs, out_specs, out_shape, scratch=(), aliases=None, vmem_mb=48,
           carry=None):
    params = pltpu.CompilerParams(dimension_semantics=("arbitrary",) * len(grid),
                                  vmem_limit_bytes=vmem_mb << 20)
    if carry is None:
        return pl.pallas_call(
            body, name=name, grid=grid, in_specs=in_specs, out_specs=out_specs, out_shape=out_shape,
            scratch_shapes=list(scratch), input_output_aliases=aliases or {}, compiler_params=params)
    single = not isinstance(out_shape, (list, tuple))
    out_specs_l = [out_specs] if single else list(out_specs)
    out_shape_l = [out_shape] if single else list(out_shape)
    n_in, n_out, n_sc = len(in_specs), len(out_shape_l), len(scratch)
    k_in, k_out = len(carry.ins), len(carry.outs)
    nsteps = int(np.prod(grid))
    assert nsteps >= 3

    def wrapped(*refs):
        ins, cins = refs[:n_in], refs[n_in:n_in + k_in]
        o0 = n_in + k_in
        outs, couts = refs[o0:o0 + n_out], refs[o0 + n_out:o0 + n_out + k_out]
        s0 = o0 + n_out + k_out
        sc, csc = refs[s0:s0 + n_sc], refs[s0 + n_sc:]
        step = pl.program_id(0)
        for ax in range(1, len(grid)):
            step = step * grid[ax] + pl.program_id(ax)
        start, mid, end = carry.phases(cins, couts, csc)
        pl.when(step == 0)(start)
        body(*ins, *outs, *sc)
        if mid is not None:
            pl.when(step == (2 * nsteps) // 3)(mid)
        pl.when(step == nsteps - 1)(end)

    all_aliases = dict(aliases or {})
    all_aliases.update({n_in + i: n_out + o for i, o in carry.aliases.items()})
    call = pl.pallas_call(
        wrapped, name=name, grid=grid, in_specs=list(in_specs) + [ANY] * k_in,
        out_specs=out_specs_l + [ANY] * k_out, out_shape=out_shape_l + list(carry.outs),
        scratch_shapes=list(scratch) + list(carry.scratch), input_output_aliases=all_aliases,
        compiler_params=params)

    def run(*args):
        res = call(*args, *carry.ins)
        core = res[:n_out]
        return (core[0] if single else list(core)), list(res[n_out:])

    return run


def _full(shape):
    nd = len(shape)
    return pl.BlockSpec(shape, lambda *_: (0,) * nd)


ANY = pl.BlockSpec(memory_space=pl.ANY)


def _mm(a, b, *, name, mode="nn", out_dtype=F32, tm, tn, tk):
    if mode == "nn":
        (m, k), (k2, n) = a.shape, b.shape
    elif mode == "nt":
        (m, k), (n, k2) = a.shape, b.shape
    else:
        (k, m), (k2, n) = a.shape, b.shape
    assert k == k2 and m % tm == 0 and n % tn == 0 and k % tk == 0, (name, a.shape, b.shape)
    nk = k // tk
    dims = {"nn": NN, "nt": NT, "tn": TN}[mode]

    def body(a_ref, b_ref, o_ref, acc):
        kk = pl.program_id(2)

        @pl.when(kk == 0)
        def _():
            acc[...] = jnp.zeros_like(acc)

        acc[...] += _bdot(a_ref[...], b_ref[...], dims)

        @pl.when(kk == nk - 1)
        def _():
            o_ref[...] = acc[...].astype(out_dtype)

    a_spec = (pl.BlockSpec((tk, tm), lambda i, j, kk: (kk, i)) if mode == "tn"
              else pl.BlockSpec((tm, tk), lambda i, j, kk: (i, kk)))
    b_spec = (pl.BlockSpec((tn, tk), lambda i, j, kk: (j, kk)) if mode == "nt"
              else pl.BlockSpec((tk, tn), lambda i, j, kk: (kk, j)))
    return _pcall(body, name=name, grid=(m // tm, n // tn, nk), in_specs=[a_spec, b_spec],
                  out_specs=pl.BlockSpec((tm, tn), lambda i, j, kk: (i, j)),
                  out_shape=jax.ShapeDtypeStruct((m, n), out_dtype),
                  scratch=[pltpu.VMEM((tm, tn), F32)])(a, b)


NT_IN = NCOL // 256


def _src_block(j):
    return j + jnp.where(j < 4, 2, jnp.where(j < 6, 3, jnp.where(j < 8, -6, jnp.where(
        j < 16, 5, jnp.where(j < 20, -7, -14)))))


def _mm_in(h, wt, tm, carry=None):
    tt = h.shape[0]

    def body(h_ref, w_ref, o_ref):
        o_ref[...] = _bdot(h_ref[...], w_ref[...], NT)

    return _pcall(body, name="mm_in", grid=(tt // tm, NT_IN),
                  in_specs=[pl.BlockSpec((tm, D), lambda i, j: (i, 0)),
                            pl.BlockSpec((256, D), lambda i, j: (_src_block(j), 0))],
                  out_specs=pl.BlockSpec((tm, 256), lambda i, j: (i, j)),
                  out_shape=jax.ShapeDtypeStruct((tt, NCOL), F32), carry=carry)(h, wt)


def _mm_dh(dp, wt, tm):
    tt = dp.shape[0]

    def body(d_ref, w_ref, o_ref, acc):
        kk = pl.program_id(1)

        @pl.when(kk == 0)
        def _():
            acc[...] = jnp.zeros_like(acc)

        acc[...] += _bdot(d_ref[...], w_ref[...])

        @pl.when(kk == NT_IN - 1)
        def _():
            o_ref[...] = acc[...]

    return _pcall(body, name="mm_dh", grid=(tt // tm, NT_IN),
                  in_specs=[pl.BlockSpec((tm, 256), lambda i, kk: (i, kk)),
                            pl.BlockSpec((256, D), lambda i, kk: (_src_block(kk), 0))],
                  out_specs=pl.BlockSpec((tm, D), lambda i, kk: (i, 0)),
                  out_shape=jax.ShapeDtypeStruct((tt, D), F32), scratch=[pltpu.VMEM((tm, D), F32)])(dp, wt)


def _mm_gin(dp, h, tk):
    tt = dp.shape[0]
    nk = tt // tk

    def body(d_ref, h_ref, o_ref, acc):
        kk = pl.program_id(1)

        @pl.when(kk == 0)
        def _():
            acc[...] = jnp.zeros_like(acc)

        acc[...] += _bdot(d_ref[...], h_ref[...], TN)

        @pl.when(kk == nk - 1)
        def _():
            o_ref[...] = acc[...]

    return _pcall(body, name="mm_gin", grid=(NT_IN, nk),
                  in_specs=[pl.BlockSpec((tk, 256), lambda j, kk: (kk, j)),
                            pl.BlockSpec((tk, D), lambda j, kk: (kk, 0))],
                  out_specs=pl.BlockSpec((256, D), lambda j, kk: (_src_block(j), 0)),
                  out_shape=jax.ShapeDtypeStruct((NCOL, D), F32), scratch=[pltpu.VMEM((256, D), F32)])(dp, h)


def _modulate(xin, nw, ss, *, name, sel):
    rows = xin.shape[0]

    def body(x_ref, nw_ref, ss_ref, h_ref):
        x = x_ref[...]
        r = lax.rsqrt(jnp.mean(x * x, axis=-1, keepdims=True) + EPS)
        s = ss_ref[0]
        h_ref[...] = ((x * r * nw_ref[...]) * (1.0 + s[1:2]) + s[0:1]).astype(BF16)

    return _pcall(body, name=name, grid=(rows // TM,),
                  in_specs=[pl.BlockSpec((TM, D), lambda i: (i, 0)), _full((1, D)),
                            pl.BlockSpec((1, 2, D), lambda i: (sel(i), 0, 0))],
                  out_specs=pl.BlockSpec((TM, D), lambda i: (i, 0)),
                  out_shape=jax.ShapeDtypeStruct((rows, D), BF16))(xin, nw, ss)


def _norm_bwd_rows(x, dh, nw, scale):
    r = lax.rsqrt(jnp.mean(x * x, axis=-1, keepdims=True) + EPS)
    xh = x * r
    dxh = dh * ((1.0 + scale) * nw)
    dx = r * (dxh - xh * jnp.mean(dxh * xh, axis=-1, keepdims=True))
    return dx, xh


def _res1_mod2(x, ao, g1, nw2, ss2):
    s_len = x.shape[0]

    def body(x_ref, ao_ref, g_ref, nw_ref, ss_ref, x1_ref, h_ref):
        x1 = x_ref[...] + g_ref[...] * ao_ref[...]
        x1_ref[...] = x1
        r = lax.rsqrt(jnp.mean(x1 * x1, axis=-1, keepdims=True) + EPS)
        s = ss_ref[0]
        h_ref[...] = ((x1 * r * nw_ref[...]) * (1.0 + s[1:2]) + s[0:1]).astype(BF16)

    row = pl.BlockSpec((TM, D), lambda i: (i, 0))
    return _pcall(body, name="res1_mod2", grid=(s_len // TM,),
                  in_specs=[row, row, _full((1, D)), _full((1, D)), _full((1, 2, D))],
                  out_specs=[row, row],
                  out_shape=[jax.ShapeDtypeStruct((s_len, D), F32),
                             jax.ShapeDtypeStruct((s_len, D), BF16)])(x, ao, g1, nw2, ss2)


TS = 512


def _acc_call(body, *, name, grid, in_specs, out_specs, out_shape, acc_shapes, args):
    return _pcall(body, name=name, grid=grid, in_specs=in_specs, out_specs=out_specs, out_shape=out_shape,
                  scratch=[pltpu.VMEM(s, F32) for s in acc_shapes])(*args)


def _mm_cs(a, w4, *, name):
    m, k = a.shape
    _, _, ns = w4.shape

    def body(a_ref, w_ref, o_ref):
        o_ref[...] = _bdot(a_ref[...], w_ref[0])

    return _pcall(body, name=name, grid=(m // TS, 4),
                  in_specs=[pl.BlockSpec((TS, k), lambda i, j: (i, 0)),
                            pl.BlockSpec((1, k, ns), lambda i, j: (j, 0, 0))],
                  out_specs=pl.BlockSpec((TS, ns), lambda i, j: (i, j)),
                  out_shape=jax.ShapeDtypeStruct((m, 4 * ns), F32))(a, w4)


def _mm_cs_nt(a, w4, *, name):
    m = a.shape[0]
    _, k, ns = w4.shape

    def body(a_ref, w_ref, o_ref, acc):
        j = pl.program_id(1)

        @pl.when(j == 0)
        def _():
            acc[...] = jnp.zeros_like(acc)

        acc[...] += _bdot(a_ref[...], w_ref[0], NT)

        @pl.when(j == 3)
        def _():
            o_ref[...] = acc[...]

    return _acc_call(body, name=name, grid=(m // TS, 4),
                     in_specs=[pl.BlockSpec((TS, ns), lambda i, j: (i, j)),
                               pl.BlockSpec((1, k, ns), lambda i, j: (j, 0, 0))],
                     out_specs=pl.BlockSpec((TS, k), lambda i, j: (i, 0)),
                     out_shape=jax.ShapeDtypeStruct((m, k), F32), acc_shapes=[(TS, k)], args=(a, w4))


def _mm_cs_tn(a, b, ns, *, name):
    s_len, k = a.shape
    nk = s_len // TS

    def body(a_ref, b_ref, o_ref, acc):
        t = pl.program_id(1)

        @pl.when(t == 0)
        def _():
            acc[...] = jnp.zeros_like(acc)

        acc[...] += _bdot(a_ref[...], b_ref[...], TN)

        @pl.when(t == nk - 1)
        def _():
            o_ref[0] = acc[...]

    return _acc_call(body, name=name, grid=(4, nk),
                     in_specs=[pl.BlockSpec((TS, k), lambda j, t: (t, 0)),
                               pl.BlockSpec((TS, ns), lambda j, t: (t, j))],
                     out_specs=pl.BlockSpec((1, k, ns), lambda j, t: (j, 0, 0)),
                     out_shape=jax.ShapeDtypeStruct((4, k, ns), F32), acc_shapes=[(k, ns)], args=(a, b))


def _ffn_up(h2, g4, u4):
    s_len = h2.shape[0]
    ns = g4.shape[1]

    def body(h_ref, g_ref, u_ref, a_ref, b_ref, z_ref):
        h = h_ref[...]
        a = _bdot(h, g_ref[0], NT)
        b = _bdot(h, u_ref[0], NT)
        a_ref[0] = a
        b_ref[0] = b
        z_ref[0] = (a * _sig(a) * b).astype(BF16)

    w = pl.BlockSpec((1, ns, D), lambda i, j: (j, 0, 0))
    o = pl.BlockSpec((1, TS, ns), lambda i, j: (j, i, 0))
    f = jax.ShapeDtypeStruct((4, s_len, ns), F32)
    return _pcall(body, name="ffn_up", grid=(s_len // TS, 4),
                  in_specs=[pl.BlockSpec((TS, D), lambda i, j: (i, 0)), w, w], out_specs=[o, o, o],
                  out_shape=[f, f, jax.ShapeDtypeStruct((4, s_len, ns), BF16)])(h2, g4, u4)


def _ffn_down(z4, dn4):
    _, s_len, ns = z4.shape

    def body(z_ref, w_ref, o_ref, acc):
        j = pl.program_id(1)

        @pl.when(j == 0)
        def _():
            acc[...] = jnp.zeros_like(acc)

        acc[...] += _bdot(z_ref[0], w_ref[0])

        @pl.when(j == 3)
        def _():
            o_ref[...] = acc[...]

    return _acc_call(body, name="ffn_down", grid=(s_len // TS, 4),
                     in_specs=[pl.BlockSpec((1, TS, ns), lambda i, j: (j, i, 0)),
                               pl.BlockSpec((1, ns, D), lambda i, j: (j, 0, 0))],
                     out_specs=pl.BlockSpec((TS, D), lambda i, j: (i, 0)),
                     out_shape=jax.ShapeDtypeStruct((s_len, D), F32), acc_shapes=[(TS, D)], args=(z4, dn4))


def _ffn_dz(dyb, dn4, a4, b4):
    _, s_len, ns = a4.shape

    def body(dy_ref, w_ref, a_ref, b_ref, da_ref, db_ref):
        dz = _bdot(dy_ref[...], w_ref[0], NT)
        a = a_ref[0]
        s = _sig(a)
        da_ref[0] = (dz * b_ref[0] * (s * (1.0 + a * (1.0 - s)))).astype(BF16)
        db_ref[0] = (dz * (a * s)).astype(BF16)

    t = pl.BlockSpec((1, TS, ns), lambda i, j: (j, i, 0))
    o = jax.ShapeDtypeStruct((4, s_len, ns), BF16)
    return _pcall(body, name="ffn_dz", grid=(s_len // TS, 4),
                  in_specs=[pl.BlockSpec((TS, D), lambda i, j: (i, 0)),
                            pl.BlockSpec((1, ns, D), lambda i, j: (j, 0, 0)), t, t],
                  out_specs=[t, t], out_shape=[o, o])(dyb, dn4, a4, b4)


def _ffn_gdn(z4, dyb):
    _, s_len, ns = z4.shape
    nk = s_len // TS

    def body(z_ref, dy_ref, o_ref, acc):
        t = pl.program_id(1)

        @pl.when(t == 0)
        def _():
            acc[...] = jnp.zeros_like(acc)

        acc[...] += _bdot(z_ref[0], dy_ref[...], TN)

        @pl.when(t == nk - 1)
        def _():
            o_ref[0] = acc[...]

    return _acc_call(body, name="ffn_gdn", grid=(4, nk),
                     in_specs=[pl.BlockSpec((1, TS, ns), lambda j, t: (j, t, 0)),
                               pl.BlockSpec((TS, D), lambda j, t: (t, 0))],
                     out_specs=pl.BlockSpec((1, ns, D), lambda j, t: (j, 0, 0)),
                     out_shape=jax.ShapeDtypeStruct((4, ns, D), F32), acc_shapes=[(ns, D)], args=(z4, dyb))


def _ffn_dh2(da4, db4, g4, u4):
    _, s_len, ns = da4.shape

    def body(da_ref, db_ref, g_ref, u_ref, o_ref, acc):
        j = pl.program_id(1)

        @pl.when(j == 0)
        def _():
            acc[...] = jnp.zeros_like(acc)

        acc[...] += _bdot(da_ref[0], g_ref[0]) + _bdot(db_ref[0], u_ref[0])

        @pl.when(j == 3)
        def _():
            o_ref[...] = acc[...]

    t = pl.BlockSpec((1, TS, ns), lambda i, j: (j, i, 0))
    w = pl.BlockSpec((1, ns, D), lambda i, j: (j, 0, 0))
    return _acc_call(body, name="ffn_dh2", grid=(s_len // TS, 4), in_specs=[t, t, w, w],
                     out_specs=pl.BlockSpec((TS, D), lambda i, j: (i, 0)),
                     out_shape=jax.ShapeDtypeStruct((s_len, D), F32), acc_shapes=[(TS, D)],
                     args=(da4, db4, g4, u4))


def _ffn_ggu(h2, da4, db4):
    _, s_len, ns = da4.shape
    nk = s_len // TS

    def body(h_ref, da_ref, db_ref, gg_ref, gu_ref, acc_g, acc_u):
        t = pl.program_id(1)

        @pl.when(t == 0)
        def _():
            acc_g[...] = jnp.zeros_like(acc_g)
            acc_u[...] = jnp.zeros_like(acc_u)

        h = h_ref[...]
        acc_g[...] += _bdot(da_ref[0], h, TN)
        acc_u[...] += _bdot(db_ref[0], h, TN)

        @pl.when(t == nk - 1)
        def _():
            gg_ref[0] = acc_g[...]
            gu_ref[0] = acc_u[...]

    d = pl.BlockSpec((1, TS, ns), lambda j, t: (j, t, 0))
    o = pl.BlockSpec((1, ns, D), lambda j, t: (j, 0, 0))
    f = jax.ShapeDtypeStruct((4, ns, D), F32)
    return _acc_call(body, name="ffn_ggu", grid=(4, nk),
                     in_specs=[pl.BlockSpec((TS, D), lambda j, t: (t, 0)), d, d], out_specs=[o, o],
                     out_shape=[f, f], acc_shapes=[(ns, D), (ns, D)], args=(h2, da4, db4))


def _loss_head(x1, y, g2, tgt):
    s_len = x1.shape[0]

    def body(x1_ref, y_ref, g_ref, t_ref, sq_ref, dx2_ref, dyb_ref, dg_ref):
        i = pl.program_id(0)

        @pl.when(i == 0)
        def _():
            sq_ref[...] = jnp.zeros_like(sq_ref)
            dg_ref[...] = jnp.zeros_like(dg_ref)

        y_ = y_ref[...]
        g = g_ref[...]
        e = x1_ref[...] + g * y_ - t_ref[...]
        sq_ref[...] += jnp.sum(e * e, axis=0, keepdims=True)
        dx2 = e * (1.0 / D)
        dx2_ref[...] = dx2
        dyb_ref[...] = (g * dx2).astype(BF16)
        dg_ref[...] += jnp.sum(dx2 * y_, axis=0, keepdims=True)

    row = pl.BlockSpec((TM, D), lambda i: (i, 0))
    vec = _full((1, D))
    return _pcall(body, name="loss_head", grid=(s_len // TM,),
                  in_specs=[row, row, vec, row], out_specs=[vec, row, row, vec],
                  out_shape=[jax.ShapeDtypeStruct((1, D), F32), jax.ShapeDtypeStruct((s_len, D), F32),
                             jax.ShapeDtypeStruct((s_len, D), BF16),
                             jax.ShapeDtypeStruct((1, D), F32)])(x1, y, g2, tgt)


def _mod2_bwd(x1, dh2, dx2, ao, nw2, ss2, g1):
    s_len = x1.shape[0]

    def body(x1_ref, dh_ref, dx2_ref, ao_ref, nw_ref, ss_ref, g_ref,
             dx1_ref, da_ref, dss_ref, dnw_ref, dg_ref):
        i = pl.program_id(0)

        @pl.when(i == 0)
        def _():
            dss_ref[...] = jnp.zeros_like(dss_ref)
            dnw_ref[...] = jnp.zeros_like(dnw_ref)
            dg_ref[...] = jnp.zeros_like(dg_ref)

        dh = dh_ref[...]
        nw = nw_ref[...]
        scale = ss_ref[0][1:2]
        dxn, xh = _norm_bwd_rows(x1_ref[...], dh, nw, scale)
        dx1 = dx2_ref[...] + dxn
        dx1_ref[...] = dx1
        da_ref[...] = (g_ref[...] * dx1).astype(BF16)
        dg_ref[...] += jnp.sum(dx1 * ao_ref[...], axis=0, keepdims=True)
        dsh = jnp.sum(dh, axis=0, keepdims=True)
        dsc = jnp.sum(dh * xh * nw, axis=0, keepdims=True)
        dss_ref[...] += jnp.concatenate([dsh, dsc], axis=0)
        dnw_ref[...] += jnp.sum(dh * xh * (1.0 + scale), axis=0, keepdims=True)

    row = pl.BlockSpec((TM, D), lambda i: (i, 0))
    vec = _full((1, D))
    return _pcall(body, name="mod2_bwd", grid=(s_len // TM,),
                  in_specs=[row, row, row, row, vec, _full((1, 2, D)), vec],
                  out_specs=[row, row, _full((2, D)), vec, vec],
                  out_shape=[jax.ShapeDtypeStruct((s_len, D), F32), jax.ShapeDtypeStruct((s_len, D), BF16),
                             jax.ShapeDtypeStruct((2, D), F32), jax.ShapeDtypeStruct((1, D), F32),
                             jax.ShapeDtypeStruct((1, D), F32)])(x1, dh2, dx2, ao, nw2, ss2, g1)


def _mod1_bwd(tok, dh, dx1, nw1, ss1, carry=None):
    tt = tok.shape[0]
    s_len = dx1.shape[0]

    def body(t_ref, dh_ref, dx1_ref, nw_ref, ss_ref, dx_ref, dss_ref, dnw_ref):
        i = pl.program_id(0)

        @pl.when(i == 0)
        def _():
            dnw_ref[...] = jnp.zeros_like(dnw_ref)

        @pl.when(i <= 1)
        def _():
            dss_ref[...] = jnp.zeros_like(dss_ref)

        dh_ = dh_ref[...]
        nw = nw_ref[...]
        scale = ss_ref[0][1:2]
        dxn, xh = _norm_bwd_rows(t_ref[...], dh_, nw, scale)

        @pl.when(i >= 1)
        def _():
            dx_ref[...] = dx1_ref[...] + dxn

        dsh = jnp.sum(dh_, axis=0, keepdims=True)
        dsc = jnp.sum(dh_ * xh * nw, axis=0, keepdims=True)
        dss_ref[...] += jnp.concatenate([dsh, dsc], axis=0)[None]
        dnw_ref[...] += jnp.sum(dh_ * xh * (1.0 + scale), axis=0, keepdims=True)

    row = pl.BlockSpec((TM, D), lambda i: (i, 0))
    lat = pl.BlockSpec((TM, D), lambda i: (jnp.maximum(i - 1, 0), 0))
    sel = pl.BlockSpec((1, 2, D), lambda i: (jnp.minimum(i, 1), 0, 0))
    return _pcall(body, name="mod1_bwd", grid=(tt // TM,),
                  in_specs=[row, row, lat, _full((1, D)), sel],
                  out_specs=[lat, sel, _full((1, D))],
                  out_shape=[jax.ShapeDtypeStruct((s_len, D), F32), jax.ShapeDtypeStruct((2, 2, D), F32),
                             jax.ShapeDtypeStruct((1, D), F32)], carry=carry)(tok, dh, dx1, nw1, ss1)


def _rows(c):
    return slice(c * CH, (c + 1) * CH)


def _chunk_masks(rev, transpose=False):
    r = lax.broadcasted_iota(jnp.int32, (TM, TM), 0)
    c = lax.broadcasted_iota(jnp.int32, (TM, TM), 1)
    same = (r // CH) == (c // CH)
    before = (c >= r) if (rev != transpose) else (c <= r)
    return same & before, same


def _hgrn_gate(fl, qraw, lg):
    lb = 1.0 / (1.0 + jnp.exp(lg[1:2] - lg[0:1]))
    sg = _sig(fl)
    f = lb + (1.0 - lb) * sg
    q = qraw * _sig(qraw) * (HGD ** -0.5)
    return lb, sg, f, q


def _hgrn_fwd(p, lg, *, rev, carry=None):
    tt = p.shape[0]
    nt = tt // TM
    ncht = TM // CH
    d = 1 if rev else 0

    def tile_of(s):
        return jnp.where(s == 0, 0, nt - s) if rev else s

    def body(f_ref, inp_ref, q_ref, lg_ref, o_ref, st_ref, state):
        s = pl.program_id(0)

        @pl.when(s == 0)
        def _():
            state[...] = jnp.zeros_like(state)

        _, _, f, q = _hgrn_gate(f_ref[...], q_ref[...], lg_ref[0])
        lf = jnp.log(f)
        causal, same = _chunk_masks(rev)
        cum = _dot(causal.astype(F32), lf, prec=HI)
        tot = _dot(same.astype(F32), lf, prec=HI)
        qd = (q * jnp.exp(cum)).astype(BF16)
        kd = ((1.0 - f) * jnp.exp(-cum)).astype(BF16)
        ke = ((1.0 - f) * jnp.exp(tot - cum)).astype(BF16)
        et = jnp.exp(tot)
        v = inp_ref[...].astype(BF16)
        order = range(ncht - 1, -1, -1) if rev else range(ncht)
        outs = []
        for h in range(4):
            sl = slice(h * HGD, (h + 1) * HGD)
            qd_, kd_, ke_, v_ = qd[:, sl], kd[:, sl], ke[:, sl], v[:, sl]
            pm = jnp.where(causal, _dot(qd_, kd_, NT), 0.0).astype(BF16)
            o_h = _dot(pm, v_)
            upd = [_dot(v_[_rows(c)], ke_[_rows(c)], TN) for c in range(ncht)]
            st = state[h]
            for c in order:
                st_ref[c, h] = st
                st = st * et[c * CH:c * CH + 1, sl] + upd[c]
            state[h] = st
            inter = [_dot(qd_[_rows(c)], st_ref[c, h].astype(BF16), NT) for c in range(ncht)]
            outs.append(o_h + jnp.concatenate(inter, axis=0))
        o_ref[...] = jnp.concatenate(outs, axis=1)

    def col(cb):
        return pl.BlockSpec((TM, HGW), lambda s: (tile_of(s), cb))

    return _pcall(
        body, name="hgrn_fwd_rev" if rev else "hgrn_fwd", grid=(nt,),
        in_specs=[col(C_FB if rev else C_FF), col(C_INP), col(C_QHG),
                  pl.BlockSpec((1, 2, HGW), lambda s: (d, 0, 0))],
        out_specs=[pl.BlockSpec((TM, HGW), lambda s: (tile_of(s), 0)),
                   pl.BlockSpec((ncht, 4, HGD, HGD), lambda s: (tile_of(s), 0, 0, 0))],
        out_shape=[jax.ShapeDtypeStruct((tt, HGW), F32),
                   jax.ShapeDtypeStruct((nt * ncht, 4, HGD, HGD), F32)],
        scratch=[pltpu.VMEM((4, HGD, HGD), F32)], carry=carry)(p, p, p, lg)


def _hgrn_bwd(p, lg, do, st, dp, prev, *, rev, carry=None):
    tt = p.shape[0]
    nt = tt // TM
    ncht = TM // CH
    d = 1 if rev else 0
    second = prev is not None

    def tile_of(s):
        return jnp.where(s == nt - 1, 0, s + 1) if rev else nt - 1 - s

    def body(*refs):
        if second:
            (f_ref, inp_ref, q_ref, lg_ref, do_ref, st_ref, dvp_ref, dqp_ref, _dp_in,
             dp_ref, dlg_ref, dstate) = refs
        else:
            (f_ref, inp_ref, q_ref, lg_ref, do_ref, st_ref, _dp_in,
             dp_ref, dv_ref, dq_ref, dlg_ref, dstate) = refs
        s = pl.program_id(0)
        tile = tile_of(s)

        @pl.when(s == 0)
        def _():
            dstate[...] = jnp.zeros_like(dstate)
            dlg_ref[...] = jnp.zeros_like(dlg_ref)

        qraw = q_ref[...]
        lb, sg, f, q = _hgrn_gate(f_ref[...], qraw, lg_ref[0])
        lf = jnp.log(f)
        causal, same = _chunk_masks(rev)
        causal_t, _ = _chunk_masks(rev, transpose=True)
        cum = _dot(causal.astype(F32), lf, prec=HI)
        tot = _dot(same.astype(F32), lf, prec=HI)
        ea, eb, ee, et = jnp.exp(cum), jnp.exp(-cum), jnp.exp(tot - cum), jnp.exp(tot)
        qdf, kdf, kef = q * ea, (1.0 - f) * eb, (1.0 - f) * ee
        qd, kd, ke = qdf.astype(BF16), kdf.astype(BF16), kef.astype(BF16)
        v = inp_ref[...].astype(BF16)
        dob = jnp.where(tile == 0, 0.0, do_ref[...]).astype(BF16)
        order = range(ncht) if rev else range(ncht - 1, -1, -1)
        dq_l, dk_l, dv_l, dcum_l, dtot_l = [], [], [], [], []
        for h in range(4):
            sl = slice(h * HGD, (h + 1) * HGD)
            qd_, kd_, ke_, v_, do_ = qd[:, sl], kd[:, sl], ke[:, sl], v[:, sl], dob[:, sl]
            pmt = jnp.where(causal_t, _dot(kd_, qd_, NT), 0.0).astype(BF16)
            dpm = jnp.where(causal, _dot(do_, v_, NT), 0.0).astype(BF16)
            dpmt = jnp.where(causal_t, _dot(v_, do_, NT), 0.0).astype(BF16)
            dv = _dot(pmt, do_)
            dqd = _dot(dpm, kd_)
            dkd = _dot(dpmt, qd_)
            upd = [_dot(do_[_rows(c)], qd_[_rows(c)], TN) for c in range(ncht)]
            ds = dstate[h]
            ds1 = [None] * ncht
            for c in order:
                ds1[c] = ds
                ds = ds * et[c * CH:c * CH + 1, sl] + upd[c]
            dstate[h] = ds
            dke_c, dv_c, dqd_c, dtot_c = [], [], [], []
            for c in range(ncht):
                st0 = st_ref[c, h]
                dsb = ds1[c].astype(BF16)
                dke_ = _dot(v_[_rows(c)], dsb)
                dke_c.append(dke_)
                dv_c.append(_dot(ke_[_rows(c)], dsb, NT))
                dqd_c.append(_dot(do_[_rows(c)], st0.astype(BF16)))
                dt = (jnp.sum(ds1[c] * st0, axis=0, keepdims=True) * et[c * CH:c * CH + 1, sl]
                      + jnp.sum(dke_ * kef[_rows(c), sl], axis=0, keepdims=True))
                dtot_c.append(jnp.broadcast_to(dt, (CH, HGD)))
            dke = jnp.concatenate(dke_c, axis=0)
            dqd = dqd + jnp.concatenate(dqd_c, axis=0)
            dv_l.append(dv + jnp.concatenate(dv_c, axis=0))
            dtot_l.append(jnp.concatenate(dtot_c, axis=0))
            dq_l.append(dqd * ea[:, sl])
            dk_l.append(dkd * eb[:, sl] + dke * ee[:, sl])
            dcum_l.append(dqd * qdf[:, sl] - dkd * kdf[:, sl] - dke * kef[:, sl])
        dcum = jnp.concatenate(dcum_l, axis=1)
        dlf = _dot(causal_t.astype(F32), dcum, prec=HI) + jnp.concatenate(dtot_l, axis=1)
        dq_t = jnp.concatenate(dq_l, axis=1)
        dv_t = jnp.concatenate(dv_l, axis=1)

        df = dlf / f - jnp.concatenate(dk_l, axis=1)
        dfl = df * (1.0 - lb) * sg * (1.0 - sg)
        dlb = jnp.sum(df * (1.0 - sg), axis=0, keepdims=True)
        dl0 = dlb * lb * (1.0 - lb)
        dlg_ref[...] += jnp.concatenate([dl0, -dl0], axis=0)[None]
        if second:
            sq = _sig(qraw)
            dqr = (dqp_ref[...] + dq_t) * (HGD ** -0.5) * (sq * (1.0 + qraw * (1.0 - sq)))
            dp_ref[...] = jnp.concatenate([dfl, dvp_ref[...] + dv_t, dqr], axis=1).astype(BF16)
        else:
            dp_ref[...] = dfl.astype(BF16)
            dv_ref[...] = dv_t
            dq_ref[...] = dq_t

    def col(cb):
        return pl.BlockSpec((TM, HGW), lambda s: (tile_of(s), cb))

    tok = pl.BlockSpec((TM, HGW), lambda s: (tile_of(s), 0))
    in_specs = [col(C_FB if rev else C_FF), col(C_INP), col(C_QHG),
                pl.BlockSpec((1, 2, HGW), lambda s: (d, 0, 0)),
                pl.BlockSpec((TM, HGW), lambda s: (jnp.maximum(tile_of(s) - 1, 0), 0)),
                pl.BlockSpec((ncht, 4, HGD, HGD), lambda s: (tile_of(s), 0, 0, 0))]
    args = [p, p, p, lg, do, st]
    dlg_spec = _full((1, 2, HGW))
    dlg_shape = jax.ShapeDtypeStruct((1, 2, HGW), F32)
    if second:
        in_specs += [tok, tok]
        args += [prev[0], prev[1]]
        out_specs = [pl.BlockSpec((TM, 3 * HGW), lambda s: (tile_of(s), 0)), dlg_spec]
        out_shape = [jax.ShapeDtypeStruct(dp.shape, BF16), dlg_shape]
    else:
        out_specs = [pl.BlockSpec((TM, HGW), lambda s: (tile_of(s), C_FB if rev else C_FF)), tok, tok, dlg_spec]
        out_shape = [jax.ShapeDtypeStruct(dp.shape, BF16), jax.ShapeDtypeStruct((tt, HGW), F32),
                     jax.ShapeDtypeStruct((tt, HGW), F32), dlg_shape]
    in_specs.append(ANY)
    args.append(dp)
    return _pcall(body, name="hgrn_bwd_rev" if rev else "hgrn_bwd", grid=(nt,),
                  in_specs=in_specs, out_specs=out_specs, out_shape=out_shape,
                  scratch=[pltpu.VMEM((4, HGD, HGD), F32)],
                  aliases={len(args) - 1: 0}, carry=carry)(*args)


def _head_rms(o, w, nheads):
    outs = []
    for h in range(nheads):
        oh = o[:, h * HGD:(h + 1) * HGD]
        outs.append(oh * lax.rsqrt(jnp.mean(oh * oh, axis=-1, keepdims=True) + EPS))
    return jnp.concatenate(outs, axis=1)


def _readout(o0, o1, p, hw4):
    s_len = o0.shape[0] - L

    def body(o0_ref, o1_ref, g_ref, w_ref, y_ref):
        xh = _head_rms(o0_ref[...] + o1_ref[...], None, 4)
        g = g_ref[...]
        y_ref[...] = (xh * w_ref[...] * (g * _sig(g))).astype(BF16)

    lat = pl.BlockSpec((TM, HGW), lambda i: (i + 1, 0))
    return _pcall(body, name="readout", grid=(s_len // TM,),
                  in_specs=[lat, lat, pl.BlockSpec((TM, HGW), lambda i: (i + 1, C_GHG)), _full((1, HGW))],
                  out_specs=pl.BlockSpec((TM, HGW), lambda i: (i, 0)),
                  out_shape=jax.ShapeDtypeStruct((s_len, HGW), BF16))(o0, o1, p, hw4)


def _readout_bwd(o0, o1, p, hw4, dy, dp, carry=None):
    tt = o0.shape[0]
    s_len = tt - L

    def body(o0_ref, o1_ref, g_ref, w_ref, dy_ref, _dp_in, dp_ref, do_ref, dw_ref):
        i = pl.program_id(0)

        @pl.when(i == 0)
        def _():
            dw_ref[...] = jnp.zeros_like(dw_ref)
            dp_ref[...] = jnp.zeros_like(dp_ref)

        @pl.when(i >= 1)
        def _():
            o = o0_ref[...] + o1_ref[...]
            g = g_ref[...]
            w = w_ref[...]
            sg = _sig(g)
            dy_ = dy_ref[...]
            dsw = dy_ * (g * sg)
            outs, xhs = [], []
            for h in range(4):
                sl = slice(h * HGD, (h + 1) * HGD)
                oh = o[:, sl]
                r = lax.rsqrt(jnp.mean(oh * oh, axis=-1, keepdims=True) + EPS)
                xh = oh * r
                dxh = dsw[:, sl] * w[:, sl]
                outs.append(r * (dxh - xh * jnp.mean(dxh * xh, axis=-1, keepdims=True)))
                xhs.append(xh)
            xh = jnp.concatenate(xhs, axis=1)
            do_ref[...] = jnp.concatenate(outs, axis=1)
            dp_ref[...] = (dy_ * xh * w * (sg * (1.0 + g * (1.0 - sg)))).astype(BF16)
            dw_ref[...] += jnp.sum(dsw * xh, axis=0, keepdims=True)

    tok = pl.BlockSpec((TM, HGW), lambda i: (i, 0))
    lat = pl.BlockSpec((TM, HGW), lambda i: (jnp.maximum(i - 1, 0), 0))
    return _pcall(body, name="readout_bwd", grid=(tt // TM,),
                  in_specs=[tok, tok, pl.BlockSpec((TM, HGW), lambda i: (i, C_GHG)), _full((1, HGW)), lat, ANY],
                  out_specs=[pl.BlockSpec((TM, HGW), lambda i: (i, C_GHG)), lat, _full((1, HGW))],
                  out_shape=[jax.ShapeDtypeStruct(dp.shape, BF16), jax.ShapeDtypeStruct((s_len, HGW), F32),
                             jax.ShapeDtypeStruct((1, HGW), F32)],
                  aliases={5: 0}, carry=carry)(o0, o1, p, hw4, dy, dp)


def _rope_tables(s_len):
    t = np.arange(s_len)
    inv = ROPE_THETA ** (-np.arange(0, 32, 2, dtype=np.float64) / 32)
    def half(pos):
        ang = pos[:, None].astype(np.float64) * inv[None, :]
        return (np.concatenate([np.cos(ang), np.cos(ang)], 1), np.concatenate([-np.sin(ang), np.sin(ang)], 1))
    cr, sr = half(t // GRID_W)
    cc, sc = half(t % GRID_W)
    cos = np.concatenate([cr, cc, cr, cc], 1)
    sin = np.concatenate([sr, sc, sr, sc], 1)
    cos = np.concatenate([np.ones((L, 128)), cos], 0)
    sin = np.concatenate([np.zeros((L, 128)), sin], 0)
    return jnp.asarray(cos, F32), jnp.asarray(sin, F32)


def _blockdiag(n, w):
    i = np.arange(n)
    return jnp.asarray((i[:, None] // w == i[None, :] // w) / float(w), F32)


def _dup_matrix():
    m = np.zeros((128, 512), np.float32)
    for g in range(2):
        for j in range(4):
            for dd in range(HDIM):
                m[64 * g + dd, 256 * g + 64 * j + dd] = 1.0
    return m


def _rot(x):
    n = x.shape[1]
    lane = lax.broadcasted_iota(jnp.int32, x.shape, 1)
    return jnp.where((lane % 32) < 16, pltpu.roll(x, n - 16, 1), pltpu.roll(x, 16, 1))


def _qk_prep(p, cos, sin, qnw8, knw2, bd512, bd128, dup):
    tt = p.shape[0]

    def body(q_ref, kv_ref, cos_ref, sin_ref, qw_ref, kw_ref, b5_ref, b1_ref, dup_ref,
             qr_ref, k4_ref, v4_ref):
        cos_, sin_ = cos_ref[...], sin_ref[...]
        q = q_ref[...]
        qn = q * lax.rsqrt(_dot(q * q, b5_ref[...], prec=HI) + EPS) * qw_ref[...]
        cos4 = jnp.concatenate([cos_] * 4, axis=1)
        sin4 = jnp.concatenate([sin_] * 4, axis=1)
        qr_ref[...] = ((qn * cos4 + _rot(qn) * sin4) * (HDIM ** -0.5)).astype(BF16)
        kv = kv_ref[...]
        k, v = kv[:, :128], kv[:, 128:]
        kn = k * lax.rsqrt(_dot(k * k, b1_ref[...], prec=HI) + EPS) * kw_ref[...]
        kr = kn * cos_ + _rot(kn) * sin_
        k4_ref[...] = _bdot(kr, dup_ref[...]).astype(BF16)
        v4_ref[...] = _bdot(v, dup_ref[...]).astype(BF16)

    row = lambda w, cb: pl.BlockSpec((TM, w), lambda i: (i, cb))
    out = jax.ShapeDtypeStruct((tt, ATW), BF16)
    return _pcall(body, name="qk_prep", grid=(tt // TM,),
                  in_specs=[row(ATW, C_QRAW), row(256, C_KV), row(128, 0), row(128, 0),
                            _full((1, ATW)), _full((1, 128)), _full((ATW, ATW)), _full((128, 128)),
                            _full((128, ATW))],
                  out_specs=[row(ATW, 0)] * 3, out_shape=[out] * 3)(
                      p, p, cos, sin, qnw8, knw2, bd512, bd128, dup)


def _attn_masks(i, nb):
    r = lax.broadcasted_iota(jnp.int32, (BLK, 3 * BLK + L), 0)
    c = lax.broadcasted_iota(jnp.int32, (BLK, 3 * BLK + L), 1)
    kpos = (i - 1) * BLK + c
    loc = (jnp.abs(c - BLK - r) <= BLK) & (kpos >= 0) & (kpos < nb * BLK)
    return loc | (c >= 3 * BLK)


def _lane_mask(j):
    lane = lax.broadcasted_iota(jnp.int32, (1, 256), 1)
    return (lane // HDIM) == j


def _attn_specs(nb):
    blk = lambda off: pl.BlockSpec((BLK, ATW), lambda i: (jnp.clip(i + off, 0, nb - 1) + 2, 0))
    ctx = pl.BlockSpec((L, ATW), lambda i: (0, 0))
    return blk, ctx


def _attn_fwd(qr, k4, v4, sinks, carry=None):
    tt = qr.shape[0]
    s_len = tt - L
    nb = s_len // BLK

    def body(sk_ref, q_ref, kp, ko, kn, kc, vp, vo, vn, vc, y_ref, lse_ref):
        i = pl.program_id(0)
        valid = _attn_masks(i, nb)
        q = q_ref[...]
        ys, lses = [], []
        for g in range(2):
            gs = slice(256 * g, 256 * g + 256)
            kcat = jnp.concatenate([kp[:, gs], ko[:, gs], kn[:, gs], kc[:, gs]], axis=0)
            vcat = jnp.concatenate([vp[:, gs], vo[:, gs], vn[:, gs], vc[:, gs]], axis=0)
            qg = q[:, gs]
            og = jnp.zeros((BLK, 256), F32)
            lg = jnp.zeros((BLK, 256), F32)
            for j in range(4):
                lm = _lane_mask(j)
                sink = sk_ref[4 * g + j]
                s = jnp.where(valid, _dot(jnp.where(lm, qg, jnp.zeros_like(qg)), kcat, NT), -1e30)
                m = jnp.maximum(jnp.max(s, axis=-1, keepdims=True), sink)
                e = jnp.exp(s - m)
                den = jnp.sum(e, axis=-1, keepdims=True) + jnp.exp(sink - m)
                pr = e / den
                og = og + jnp.where(lm, _bdot(pr, vcat), 0.0)
                lg = lg + jnp.where(lm, m + jnp.log(den), 0.0)
            ys.append(og)
            lses.append(lg)
        y_ref[...] = jnp.concatenate(ys, axis=1).astype(BF16)
        lse_ref[...] = jnp.concatenate(lses, axis=1)

    blk, ctx = _attn_specs(nb)
    out = pl.BlockSpec((BLK, ATW), lambda i: (i, 0))
    return _pcall(body, name="attn_fwd", grid=(nb,),
                  in_specs=[pl.BlockSpec(memory_space=pltpu.SMEM), blk(0),
                            blk(-1), blk(0), blk(1), ctx, blk(-1), blk(0), blk(1), ctx],
                  out_specs=[out, out],
                  out_shape=[jax.ShapeDtypeStruct((s_len, ATW), BF16),
                             jax.ShapeDtypeStruct((s_len, ATW), F32)], carry=carry)(
                      sinks, qr, k4, k4, k4, k4, v4, v4, v4, v4)


def _attn_bwd(qr, k4, v4, sinks, y, lse, dy, carry=None):
    tt = qr.shape[0]
    s_len = tt - L
    nb = s_len // BLK

    def body(sk_ref, q_ref, kp, ko, kn, kc, vp, vo, vn, vc, y_ref, lse_ref, dy_ref,
             dq_ref, dkw_ref, dvw_ref, dkc_ref, dvc_ref, dsk_ref):
        i = pl.program_id(0)

        @pl.when(i == 0)
        def _():
            dkc_ref[...] = jnp.zeros_like(dkc_ref)
            dvc_ref[...] = jnp.zeros_like(dvc_ref)
            dsk_ref[...] = jnp.zeros_like(dsk_ref)

        valid = _attn_masks(i, nb)
        q = q_ref[...]
        dy_ = dy_ref[...]
        dly = dy_ * y_ref[...].astype(F32)
        lse_ = lse_ref[...]
        dqs = []
        for g in range(2):
            gs = slice(256 * g, 256 * g + 256)
            kcat = jnp.concatenate([kp[:, gs], ko[:, gs], kn[:, gs], kc[:, gs]], axis=0)
            vcat = jnp.concatenate([vp[:, gs], vo[:, gs], vn[:, gs], vc[:, gs]], axis=0)
            qg, dyg, dlg, lsg = q[:, gs], dy_[:, gs], dly[:, gs], lse_[:, gs]
            dqg = jnp.zeros((BLK, 256), F32)
            dkg = jnp.zeros((3 * BLK + L, 256), F32)
            dvg = jnp.zeros((3 * BLK + L, 256), F32)
            for j in range(4):
                lm = _lane_mask(j)
                sink = sk_ref[4 * g + j]
                qm = jnp.where(lm, qg, jnp.zeros_like(qg))
                dym = jnp.where(lm, dyg, 0.0).astype(BF16)
                lse_h = jnp.max(jnp.where(lm, lsg, -1e30), axis=-1, keepdims=True)
                delta = jnp.sum(jnp.where(lm, dlg, 0.0), axis=-1, keepdims=True)
                s = _dot(qm, kcat, NT)
                pr = jnp.where(valid, jnp.exp(s - lse_h), 0.0)
                dpr = _dot(dym, vcat, NT)
                dsc = pr * (dpr - delta)
                psink = jnp.exp(sink - lse_h)
                dsk_ref[4 * g + j:4 * g + j + 1, :] += jnp.broadcast_to(
                    -jnp.sum(psink * delta, axis=0, keepdims=True), (1, 128))
                dsb = dsc.astype(BF16)
                dqg = dqg + jnp.where(lm, _dot(dsb, kcat), 0.0)
                dkg = dkg + _dot(dsb, qm, TN)
                dvg = dvg + _dot(pr.astype(BF16), dym, TN)
            dqs.append(dqg)
            dkw_ref[0, :, gs] = dkg[:3 * BLK]
            dvw_ref[0, :, gs] = dvg[:3 * BLK]
            dkc_ref[:, gs] += dkg[3 * BLK:]
            dvc_ref[:, gs] += dvg[3 * BLK:]
        dq_ref[...] = jnp.concatenate(dqs, axis=1)

    blk, ctx = _attn_specs(nb)
    out = pl.BlockSpec((BLK, ATW), lambda i: (i, 0))
    win = pl.BlockSpec((1, 3 * BLK, ATW), lambda i: (i, 0, 0))
    acc = _full((L, ATW))
    return _pcall(body, name="attn_bwd", grid=(nb,),
                  in_specs=[pl.BlockSpec(memory_space=pltpu.SMEM), blk(0),
                            blk(-1), blk(0), blk(1), ctx, blk(-1), blk(0), blk(1), ctx, out, out, out],
                  out_specs=[out, win, win, acc, acc, _full((8, 128))],
                  out_shape=[jax.ShapeDtypeStruct((s_len, ATW), F32),
                             jax.ShapeDtypeStruct((nb, 3 * BLK, ATW), F32),
                             jax.ShapeDtypeStruct((nb, 3 * BLK, ATW), F32),
                             jax.ShapeDtypeStruct((L, ATW), F32), jax.ShapeDtypeStruct((L, ATW), F32),
                             jax.ShapeDtypeStruct((8, 128), F32)], carry=carry)(
                      sinks, qr, k4, k4, k4, k4, v4, v4, v4, v4, y, lse, dy)


def _attn_post(p, cos, sin, qnw8, knw2, bd512, bd128, dupt, dq, dkw, dvw, dkc, dvc, dp, carry=None):
    tt = p.shape[0]
    s_len = tt - L
    nb = s_len // BLK
    nctx = L // BLK

    def body(q_ref, kv_ref, cos_ref, sin_ref, qw_ref, kw_ref, b5_ref, b1_ref, dupt_ref,
             dq_ref, kwp, kwo, kwn, vwp, vwo, vwn, dkc_ref, dvc_ref, _dp_in,
             dp_ref, dqw_ref, dkw_ref):
        t = pl.program_id(0)
        j = t - nctx

        @pl.when(t == 0)
        def _():
            dqw_ref[...] = jnp.zeros_like(dqw_ref)
            dkw_ref[...] = jnp.zeros_like(dkw_ref)

        is_lat = t >= nctx
        cos_, sin_ = cos_ref[...], sin_ref[...]
        has_p = is_lat & (j >= 1)
        has_n = is_lat & (j <= nb - 2)
        dk4 = (jnp.where(is_lat, kwo[0], dkc_ref[...]) + jnp.where(has_p, kwp[0], 0.0)
               + jnp.where(has_n, kwn[0], 0.0))
        dv4 = (jnp.where(is_lat, vwo[0], dvc_ref[...]) + jnp.where(has_p, vwp[0], 0.0)
               + jnp.where(has_n, vwn[0], 0.0))
        dkr = _dot(dk4, dupt_ref[...], prec=HI)
        dv = _dot(dv4, dupt_ref[...], prec=HI)
        kv = kv_ref[...]
        k = kv[:, :128]
        kw = kw_ref[...]
        rk = lax.rsqrt(_dot(k * k, b1_ref[...], prec=HI) + EPS)
        xk = k * rk
        dkn = dkr * cos_ + _rot(dkr * sin_)
        dxk = dkn * kw
        dk = rk * (dxk - xk * _dot(dxk * xk, b1_ref[...], prec=HI))
        dkw_ref[...] += jnp.sum(dkn * xk, axis=0, keepdims=True)
        q = q_ref[...]
        qw = qw_ref[...]
        rq = lax.rsqrt(_dot(q * q, b5_ref[...], prec=HI) + EPS)
        xq = q * rq
        cos4 = jnp.concatenate([cos_] * 4, axis=1)
        sin4 = jnp.concatenate([sin_] * 4, axis=1)
        dqr = jnp.where(is_lat, dq_ref[...], 0.0) * (HDIM ** -0.5)
        dqn = dqr * cos4 + _rot(dqr * sin4)
        dxq = dqn * qw
        dqraw = rq * (dxq - xq * _dot(dxq * xq, b5_ref[...], prec=HI))
        dqw_ref[...] += jnp.sum(dqn * xq, axis=0, keepdims=True)
        dp_ref[...] = jnp.concatenate([dqraw, dk, dv], axis=1).astype(BF16)

    row = lambda w, cb: pl.BlockSpec((BLK, w), lambda t: (t, cb))
    lat = pl.BlockSpec((BLK, ATW), lambda t: (jnp.maximum(t - nctx, 0), 0))

    def part(off):
        return pl.BlockSpec((1, BLK, ATW), lambda t: (jnp.clip(t - nctx + off, 0, nb - 1), 1 - off, 0))

    cacc = pl.BlockSpec((BLK, ATW), lambda t: (jnp.minimum(t, nctx - 1), 0))
    return _pcall(body, name="attn_post", grid=(tt // BLK,),
                  in_specs=[row(ATW, C_QRAW), row(256, C_KV), row(128, 0), row(128, 0),
                            _full((1, ATW)), _full((1, 128)), _full((ATW, ATW)), _full((128, 128)),
                            _full((ATW, 128)), lat, part(-1), part(0), part(1), part(-1), part(0), part(1),
                            cacc, cacc, ANY],
                  out_specs=[pl.BlockSpec((BLK, 768), lambda t: (t, C_QKV)), _full((1, ATW)), _full((1, 128))],
                  out_shape=[jax.ShapeDtypeStruct(dp.shape, BF16), jax.ShapeDtypeStruct((1, ATW), F32),
                             jax.ShapeDtypeStruct((1, 128), F32)],
                  aliases={18: 0}, carry=carry)(p, p, cos, sin, qnw8, knw2, bd512, bd128, dupt,
                                   dq, dkw, dkw, dkw, dvw, dvw, dvw, dkc, dvc, dp)


def _merge(ah, aa, p):
    s_len = ah.shape[0]

    def body(ah_ref, aa_ref, gh_ref, ga_ref, m_ref):
        m_ref[...] = (_sig(gh_ref[...]) * ah_ref[...] + _sig(ga_ref[...]) * aa_ref[...]).astype(BF16)

    row = pl.BlockSpec((TM, D), lambda i: (i, 0))
    return _pcall(body, name="merge", grid=(s_len // TM,),
                  in_specs=[row, row, pl.BlockSpec((TM, D), lambda i: (i + 1, 2)),
                            pl.BlockSpec((TM, D), lambda i: (i + 1, 3))],
                  out_specs=row, out_shape=jax.ShapeDtypeStruct((s_len, D), BF16))(ah, aa, p, p)


def _merge_bwd(dm, ah, aa, p, carry=None):
    tt = p.shape[0]
    s_len = tt - L

    def body(dm_ref, ah_ref, aa_ref, gh_ref, ga_ref, dp_ref, dmh_ref, dma_ref):
        i = pl.program_id(0)

        @pl.when(i == 0)
        def _():
            dp_ref[...] = jnp.zeros_like(dp_ref)

        @pl.when(i >= 1)
        def _():
            dm_ = dm_ref[...]
            sh, sa = _sig(gh_ref[...]), _sig(ga_ref[...])
            dp_ref[...] = jnp.concatenate([dm_ * ah_ref[...] * sh * (1.0 - sh),
                                           dm_ * aa_ref[...] * sa * (1.0 - sa)], axis=1).astype(BF16)
            dmh_ref[...] = (dm_ * sh).astype(BF16)
            dma_ref[...] = (dm_ * sa).astype(BF16)

    lat = pl.BlockSpec((TM, D), lambda i: (jnp.maximum(i - 1, 0), 0))
    return _pcall(body, name="merge_bwd", grid=(tt // TM,),
                  in_specs=[lat, lat, lat, pl.BlockSpec((TM, D), lambda i: (i, 2)),
                            pl.BlockSpec((TM, D), lambda i: (i, 3))],
                  out_specs=[pl.BlockSpec((TM, 2 * D), lambda i: (i, C_GATES)), lat, lat],
                  out_shape=[jax.ShapeDtypeStruct((tt, NCOL), BF16), jax.ShapeDtypeStruct((s_len, D), BF16),
                             jax.ShapeDtypeStruct((s_len, D), BF16)], carry=carry)(dm, ah, aa, p, p)


def _local_step(x, ctx, tgt, mod, modc, nw1, nw2, lg, hw, qnw, knw, sinks,
                w_in, wts, dist=None):
    s_len = x.shape[0]
    tt = s_len + L
    tok = jnp.concatenate([ctx, x], axis=0)
    ss1 = jnp.stack([modc, mod[0:2]])
    ss2 = mod[3:5][None]
    g1, g2 = mod[2:3], mod[5:6]
    hw4 = jnp.tile(hw, (1, 4))
    qnw8 = jnp.tile(qnw, (1, 8))
    knw2 = jnp.tile(knw, (1, 2))
    cos, sin = _rope_tables(s_len)
    bd512, bd128 = _blockdiag(ATW, HDIM), _blockdiag(128, HDIM)
    dupm = _dup_matrix()
    dup, dupt = jnp.asarray(dupm, BF16), jnp.asarray(dupm.T, F32)
    tmt = tt

    def four(b):
        return b.reshape(4, 2 * b.shape[1], b.shape[2])

    def halves(g):
        return g.reshape(4, 2, g.shape[1] // 2, g.shape[2])

    h = _modulate(tok, nw1, ss1, name="mod1", sel=lambda i: jnp.minimum(i, 1))
    if dist is None:
        bh4, ba4, w_o, g4, u4, dn4 = wts
        p = _mm_in(h, w_in, tmt)
        o0, st0 = _hgrn_fwd(p, lg, rev=False)
        o1, st1 = _hgrn_fwd(p, lg, rev=True)
    else:
        core, chip = dist
        p, first = _mm_in(h, w_in, tmt, carry=_carry_gather(list(wts[0:3])))
        (o0, st0), (g8,) = _hgrn_fwd(p, lg, rev=False, carry=_carry_gather([wts[3]]))
        o1, st1 = _hgrn_fwd(p, lg, rev=True)
        bh4, ba4, w_o, g4 = four(first[0]), four(first[1]), four(first[2]).reshape(D, D), four(g8)
    y_hg = _readout(o0, o1, p, hw4)
    qr, k4, v4 = _qk_prep(p, cos, sin, qnw8, knw2, bd512, bd128, dup)
    if dist is None:
        y_at, lse = _attn_fwd(qr, k4, v4, sinks)
    else:
        (y_at, lse), (u8, dn8) = _attn_fwd(qr, k4, v4, sinks, carry=_carry_gather(list(wts[4:6])))
        u4, dn4 = four(u8), four(dn8)
    ah = _mm_cs(y_hg, bh4, name="mm_bh")
    aa = _mm_cs(y_at, ba4, name="mm_ba")
    mixed = _merge(ah, aa, p)
    ao = _mm(mixed, w_o, name="mm_o", tm=512, tn=D, tk=D)
    x1, h2 = _res1_mod2(x, ao, g1, nw2, ss2)
    a4, b4, z4 = _ffn_up(h2, g4, u4)
    y = _ffn_down(z4, dn4)
    sq, dx2, dyb, dg2 = _loss_head(x1, y, g2, tgt)

    da4, db4 = _ffn_dz(dyb, dn4, a4, b4)
    g_dn = _ffn_gdn(z4, dyb)
    dh2 = _ffn_dh2(da4, db4, g4, u4)
    g_g, g_u = _ffn_ggu(h2, da4, db4)
    dx1, dattn, dss2, dnw2, dg1 = _mod2_bwd(x1, dh2, dx2, ao, nw2, ss2, g1)
    dm = _mm(dattn, w_o, name="mm_dm", mode="nt", tm=512, tn=D, tk=D)
    g_o = _mm(mixed, dattn, name="mm_go", mode="tn", tm=D, tn=D, tk=512)
    if dist is None:
        dp, dmh, dma = _merge_bwd(dm, ah, aa, p)
    else:
        ffn_units = [halves(g_dn), halves(g_g), halves(g_u)]
        (dp, dmh, dma), ffn_recv = _merge_bwd(dm, ah, aa, p, carry=_carry_pairx(ffn_units))
        ffn_pairs = _rs_pair_add(ffn_units, ffn_recv, core)
    dy_hg = _mm_cs_nt(dmh, bh4, name="mm_dyh")
    dy_at = _mm_cs_nt(dma, ba4, name="mm_dya")
    g_bh = _mm_cs_tn(y_hg, dmh, D // 4, name="mm_gbh")
    g_ba = _mm_cs_tn(y_at, dma, D // 4, name="mm_gba")
    if dist is None:
        dp, do, dhw4 = _readout_bwd(o0, o1, p, hw4, dy_hg, dp)
        dq, dkw, dvw, dkc, dvc, dsk = _attn_bwd(qr, k4, v4, sinks, y_at, lse, dy_at)
        dp, dqnw8, dknw2 = _attn_post(p, cos, sin, qnw8, knw2, bd512, bd128, dupt, dq, dkw, dvw, dkc, dvc, dp)
    else:
        mix_units = [halves(g_bh), halves(g_ba), halves(g_o.reshape(4, D // 4, D))]
        (dp, do, dhw4), mix_recv = _readout_bwd(o0, o1, p, hw4, dy_hg, dp, carry=_carry_pairx(mix_units))
        mix_pairs = _rs_pair_add(mix_units, mix_recv, core)
        (dq, dkw, dvw, dkc, dvc, dsk), contribs_a = _attn_bwd(qr, k4, v4, sinks, y_at, lse, dy_at,
                                                              carry=_carry_chipx(ffn_pairs[0:2]))
        reds_a = _rs_chip_add(ffn_pairs[0:2], contribs_a, core, chip)
        (dp, dqnw8, dknw2), post = _attn_post(
            p, cos, sin, qnw8, knw2, bd512, bd128, dupt, dq, dkw, dvw, dkc, dvc, dp,
            carry=_carry_join(_carry_chipx(ffn_pairs[2:3]), _carry_sibx(reds_a)))
        reds_b = _rs_chip_add(ffn_pairs[2:3], post[0:1], core, chip)
    if dist is None:
        dp, dv0, dq0, dlg0 = _hgrn_bwd(p, lg, do, st0, dp, None, rev=False)
        dp, dlg1 = _hgrn_bwd(p, lg, do, st1, dp, (dv0, dq0), rev=True)
    else:
        (dp, dv0, dq0, dlg0), mid = _hgrn_bwd(p, lg, do, st0, dp, None, rev=False,
                                              carry=_carry_join(_carry_chipx(mix_pairs), _carry_sibx(reds_b)))
        mix_reds = _rs_chip_add(mix_pairs, mid[0:3], core, chip)
        (dp, dlg1), mix_done = _hgrn_bwd(p, lg, do, st1, dp, (dv0, dq0), rev=True, carry=_carry_sibx(mix_reds))
        ffn_done = post[1:3] + mid[3:4]
    dh = _mm_dh(dp, w_in, tmt)
    g_in = _mm_gin(dp, h, tmt)
    if dist is None:
        gx, dss1, dnw1 = _mod1_bwd(tok, dh, dx1, nw1, ss1)
        rs = None
    else:
        in_units = [halves(g_in.reshape(4, NCOL // 4, D))]
        (gx, dss1, dnw1), in_recv = _mod1_bwd(tok, dh, dx1, nw1, ss1, carry=_carry_pairx(in_units))
        rs = dict(ffn_done=ffn_done, mix_done=mix_done, in_units=in_units, in_recv=in_recv)

    dmod = jnp.concatenate([dss1[1], dg1, dss2, dg2], axis=0)
    dmodc = dss1[0]
    raw = (dss1, dg1, dss2, dg2, dnw1, dnw2, dhw4, dqnw8, dknw2, dsk, dlg0, dlg1)
    small = dict(raw=raw, dmod=dmod, dmodc=dmodc, dnw1=dnw1, dnw2=dnw2,
                 dhw=dhw4.reshape(4, HGD).sum(0, keepdims=True),
                 dqnw=dqnw8.reshape(8, HDIM).sum(0, keepdims=True),
                 dknw=dknw2.reshape(2, HDIM).sum(0, keepdims=True),
                 dsinks=dsk[:, 0], dlg=jnp.concatenate([dlg0, dlg1], axis=0))
    big = dict(w_in=g_in, w_bh=g_bh, w_ba=g_ba, w_o=g_o, w_g=g_g, w_u=g_u, w_dn=g_dn)
    return sq, gx, big, small, rs


def _place():
    x, y, c = lax.axis_index("x"), lax.axis_index("y"), lax.axis_index("c")
    return x, y, c


def _gather_blocks(x_refs, out_refs, send_sems, recv_sems, local_sems):
    n = len(out_refs)
    x, y, c = _place()
    me, sibling = (x, y, c), (x, y, 1 - c)
    chips = [(1 - x, y), (x, 1 - y), (1 - x, 1 - y)]

    def slot(u, px, py, pc):
        return out_refs[u].at[4 * px + 2 * py + pc]

    def copy(u, k, block, to, src=None):
        return pltpu.make_async_remote_copy(
            src_ref=slot(u, *block) if src is None else src, dst_ref=slot(u, *block),
            send_sem=send_sems.at[u, k], recv_sem=recv_sems.at[u, k], device_id=to, device_id_type=MESH)

    mines = []
    if x_refs is not None:
        mines = [pltpu.make_async_copy(x_refs[u], slot(u, *me), local_sems.at[u]) for u in range(n)]
    for cp in mines:
        cp.start()
    first = []
    for u in range(n):
        src = None if x_refs is None else x_refs[u]
        first.append(copy(u, 0, me, sibling, src=src))
        first += [copy(u, 1 + j, me, (*chip, c), src=src) for j, chip in enumerate(chips)]
    for cp in first:
        cp.start()
    passed = []
    for j, chip in enumerate(chips):
        for u in range(n):
            copy(u, 1 + j, (*chip, c), me).wait_recv()
            fwd = copy(u, 4 + j, (*chip, c), sibling)
            fwd.start()
            passed.append(fwd)
    for u in range(n):
        copy(u, 0, sibling, me).wait_recv()
    for j, chip in enumerate(chips):
        for u in range(n):
            copy(u, 4 + j, (*chip, 1 - c), me).wait_recv()
    for cp in first + passed:
        cp.wait_send()
    for cp in mines:
        cp.wait()


def _gather_sems(n):
    return [pltpu.SemaphoreType.DMA((n, 7)), pltpu.SemaphoreType.DMA((n, 7)), pltpu.SemaphoreType.DMA((n,))]


def _allgather(blks, *, name, in_vmem):
    n = len(blks)
    space = pltpu.VMEM if in_vmem else pl.ANY

    def body(*refs):
        _gather_blocks(refs[:n], refs[n:2 * n], *refs[2 * n:])

    return pl.pallas_call(
        body, name=name, out_shape=[jax.ShapeDtypeStruct((8,) + b.shape, b.dtype) for b in blks],
        in_specs=[pl.BlockSpec(memory_space=space)] * n, out_specs=[pl.BlockSpec(memory_space=space)] * n,
        scratch_shapes=_gather_sems(n))(*blks)


def _cast_place(ws, c, dev):
    n = len(ws)

    def body(s_ref, *refs):
        for u in range(n):
            refs[n + u][0] = refs[u][...].astype(BF16)

    in_specs, out_specs, out_shape = [], [], []
    for w in ws:
        q, cols = w.shape[0] // 4, w.shape[1]
        in_specs.append(pl.BlockSpec((q, cols), lambda i, s: (2 * s[0] + i, 0)))
        out_specs.append(pl.BlockSpec((1, q, cols), lambda i, s: (s[1], i, 0)))
        out_shape.append(jax.ShapeDtypeStruct((8, 2 * q, cols), BF16))
    return pl.pallas_call(
        body, name="cast_place",
        grid_spec=pltpu.PrefetchScalarGridSpec(num_scalar_prefetch=1, grid=(2,), in_specs=in_specs,
                                               out_specs=out_specs),
        out_shape=out_shape,
        compiler_params=pltpu.CompilerParams(vmem_limit_bytes=48 << 20))(jnp.stack([c, dev]), *ws)


def _gather_phases(out_refs, send_sems, recv_sems):
    n = len(out_refs)
    x, y, c = _place()
    me, sibling = (x, y, c), (x, y, 1 - c)
    chips = [(1 - x, y), (x, 1 - y), (1 - x, 1 - y)]

    def copy(u, k, block, to):
        px, py, pc = block
        ref = out_refs[u].at[4 * px + 2 * py + pc]
        return pltpu.make_async_remote_copy(src_ref=ref, dst_ref=ref, send_sem=send_sems.at[u, k],
                                            recv_sem=recv_sems.at[u, k], device_id=to, device_id_type=MESH)

    def start():
        for u in range(n):
            copy(u, 0, me, sibling).start()
            for j, chip in enumerate(chips):
                copy(u, 1 + j, me, (*chip, c)).start()

    def mid():
        for j, chip in enumerate(chips):
            for u in range(n):
                copy(u, 1 + j, (*chip, c), me).wait_recv()
                copy(u, 4 + j, (*chip, c), sibling).start()

    def end():
        for u in range(n):
            copy(u, 0, sibling, me).wait_recv()
        for j, chip in enumerate(chips):
            for u in range(n):
                copy(u, 4 + j, (*chip, 1 - c), me).wait_recv()
        for u in range(n):
            copy(u, 0, me, sibling).wait_send()
            for j, chip in enumerate(chips):
                copy(u, 1 + j, me, (*chip, c)).wait_send()
                copy(u, 4 + j, (*chip, c), sibling).wait_send()

    return start, mid, end


def _carry_gather(bufs):
    n = len(bufs)
    return _Carry(bufs, [jax.ShapeDtypeStruct(b.shape, b.dtype) for b in bufs], {u: u for u in range(n)},
                  [pltpu.SemaphoreType.DMA((n, 7)), pltpu.SemaphoreType.DMA((n, 7))],
                  lambda ins, outs, sems: _gather_phases(outs, *sems))


def _allgather_inplace(bufs, *, name):
    n = len(bufs)

    def body(*refs):
        for phase in _gather_phases(refs[n:2 * n], *refs[2 * n:]):
            phase()

    return pl.pallas_call(
        body, name=name, out_shape=[jax.ShapeDtypeStruct(b.shape, b.dtype) for b in bufs],
        in_specs=[ANY] * n, out_specs=[ANY] * n, input_output_aliases={u: u for u in range(n)},
        scratch_shapes=[pltpu.SemaphoreType.DMA((n, 7)), pltpu.SemaphoreType.DMA((n, 7))])(*bufs)


def _ag_small(raw):
    def body(dss1, dg1, dss2, dg2, dnw1, dnw2, dhw4, dqnw8, dknw2, dsk, dlg0, dlg1,
             out_ref, tot_ref, blk, send_sems, recv_sems, local_sems):
        blk[...] = jnp.zeros_like(blk)
        blk[0:2, :] = dss1[1]
        blk[2:3, :] = dg1[...]
        blk[3:5, :] = dss2[...]
        blk[5:6, :] = dg2[...]
        blk[6:8, :] = dss1[0]
        blk[8:9, :] = dnw1[...]
        blk[9:10, :] = dnw2[...]
        blk[10:11, 0:HGW] = dhw4[...]
        blk[10:11, HGW:D] = dqnw8[...]
        blk[11:12, 0:128] = dknw2[...]
        blk[12:14, 0:HGW] = dlg0[0]
        blk[14:16, 0:HGW] = dlg1[0]
        blk[16:24, 0:128] = dsk[...]
        _gather_blocks([blk], [out_ref], send_sems, recv_sems, local_sems)
        acc = out_ref[0]
        for i in range(1, 8):
            acc = acc + out_ref[i]
        tot_ref[...] = acc

    vm = pl.BlockSpec(memory_space=pltpu.VMEM)
    return pl.pallas_call(
        body, name="ag_small",
        out_shape=[jax.ShapeDtypeStruct((8, 24, D), F32), jax.ShapeDtypeStruct((24, D), F32)],
        in_specs=[vm] * 12, out_specs=[vm, vm],
        scratch_shapes=[pltpu.VMEM((24, D), F32)] + _gather_sems(1))(*raw)


def _rs_pair_exchange(units):
    n = len(units)

    def body(*refs):
        start, _, end = _pairx_phases(refs[:n], refs[n:2 * n], *refs[2 * n:])
        start()
        end()

    return pl.pallas_call(
        body, name="rs_pair_exchange", out_shape=_pairx_shapes(units),
        in_specs=[ANY] * n, out_specs=[ANY] * n,
        scratch_shapes=[pltpu.SemaphoreType.DMA((n, 4)), pltpu.SemaphoreType.DMA((n, 4))])(*units)


def _pairx_shapes(units):
    return [jax.ShapeDtypeStruct((4,) + g.shape[2:], g.dtype) for g in units]


def _pairx_phases(g_refs, r_refs, send_sems, recv_sems):
    n = len(g_refs)
    x, y, c = _place()
    cps = [pltpu.make_async_remote_copy(
        src_ref=g_refs[u].at[j, 1 - c], dst_ref=r_refs[u].at[j], send_sem=send_sems.at[u, j],
        recv_sem=recv_sems.at[u, j], device_id=(x, y, 1 - c), device_id_type=MESH)
        for u in range(n) for j in range(4)]

    def start():
        for cp in cps:
            cp.start()

    def end():
        for cp in cps:
            cp.wait()

    return start, None, end


def _carry_pairx(units):
    n = len(units)
    return _Carry(units, _pairx_shapes(units), {},
                  [pltpu.SemaphoreType.DMA((n, 4)), pltpu.SemaphoreType.DMA((n, 4))],
                  lambda ins, outs, sems: _pairx_phases(ins, outs, *sems))


def _rs_pair_add(units, recvs, c):
    n = len(units)

    def body(c_ref, *refs):
        for u in range(n):
            refs[2 * n + u][...] = (refs[u][0] + refs[n + u][...]).astype(BF16)

    in_specs, out_specs, out_shape = [], [], []
    for g in units:
        h, w = g.shape[2] // 2, g.shape[3]
        in_specs.append(pl.BlockSpec((1, 1, h, w), lambda j, i, cr: (j, cr[0], i, 0)))
    for g in units:
        h, w = g.shape[2] // 2, g.shape[3]
        in_specs.append(pl.BlockSpec((1, h, w), lambda j, i, cr: (j, i, 0)))
        out_specs.append(pl.BlockSpec((1, h, w), lambda j, i, cr: (j, i, 0)))
        out_shape.append(jax.ShapeDtypeStruct((4, 2 * h, w), BF16))
    return pl.pallas_call(
        body, name="rs_pair_add",
        grid_spec=pltpu.PrefetchScalarGridSpec(num_scalar_prefetch=1, grid=(4, 2), in_specs=in_specs,
                                               out_specs=out_specs),
        out_shape=out_shape,
        compiler_params=pltpu.CompilerParams(vmem_limit_bytes=48 << 20))(c.reshape(1), *units, *recvs)


def _rs_chip_exchange(pairs):
    n = len(pairs)

    def body(*refs):
        start, _, end = _chipx_phases(refs[:n], refs[n:2 * n], *refs[2 * n:])
        start()
        end()

    return pl.pallas_call(
        body, name="rs_chip_exchange", out_shape=[jax.ShapeDtypeStruct(p.shape, p.dtype) for p in pairs],
        in_specs=[ANY] * n, out_specs=[ANY] * n,
        scratch_shapes=[pltpu.SemaphoreType.DMA((n, 3)), pltpu.SemaphoreType.DMA((n, 3))])(*pairs)


def _chipx_phases(p_refs, r_refs, send_sems, recv_sems):
    n = len(p_refs)
    x, y, c = _place()
    k = 2 * x + y
    sends = []
    for d in range(1, 4):
        j = (k + d) % 4
        for u in range(n):
            sends.append(pltpu.make_async_remote_copy(
                src_ref=p_refs[u].at[j], dst_ref=r_refs[u].at[k], send_sem=send_sems.at[u, d - 1],
                recv_sem=recv_sems.at[u, d - 1], device_id=(j // 2, j % 2, c), device_id_type=MESH))

    def start():
        for cp in sends:
            cp.start()

    def end():
        for d in range(1, 4):
            src = (k + 4 - d) % 4
            for u in range(n):
                pltpu.make_async_remote_copy(
                    src_ref=p_refs[u].at[src], dst_ref=r_refs[u].at[src], send_sem=send_sems.at[u, d - 1],
                    recv_sem=recv_sems.at[u, d - 1], device_id=(x, y, c), device_id_type=MESH).wait_recv()
        for cp in sends:
            cp.wait_send()

    return start, None, end


def _carry_chipx(pairs):
    n = len(pairs)
    return _Carry(pairs, [jax.ShapeDtypeStruct(p.shape, p.dtype) for p in pairs], {},
                  [pltpu.SemaphoreType.DMA((n, 3)), pltpu.SemaphoreType.DMA((n, 3))],
                  lambda ins, outs, sems: _chipx_phases(ins, outs, *sems))


def _rs_chip_add(pairs, contribs, c, chip):
    n = len(pairs)

    def body(s_ref, *refs):
        for u in range(n):
            a, b, c_, d = refs[4 * u:4 * u + 4]
            refs[4 * n + u][0] = ((a[0].astype(F32) + b[0].astype(F32)) + c_[0].astype(F32)) + d[0].astype(F32)

    in_specs, out_specs, out_shape, args = [], [], [], []
    for p, r in zip(pairs, contribs):
        h, w = p.shape[1] // 2, p.shape[2]
        in_specs += [pl.BlockSpec((1, h, w), functools.partial(lambda d, i, s: ((s[1] + d) % 4, i, 0), d))
                     for d in range(4)]
        args += [p, r, r, r]
        out_specs.append(pl.BlockSpec((1, h, w), lambda i, s: (s[0], i, 0)))
        out_shape.append(jax.ShapeDtypeStruct((2, 2 * h, w), F32))
    return pl.pallas_call(
        body, name="rs_chip_add",
        grid_spec=pltpu.PrefetchScalarGridSpec(num_scalar_prefetch=1, grid=(2,), in_specs=in_specs,
                                               out_specs=out_specs),
        out_shape=out_shape,
        compiler_params=pltpu.CompilerParams(vmem_limit_bytes=48 << 20))(jnp.stack([c, chip]), *args)


def _rs_sibling_gather(reds):
    n = len(reds)

    def body(*refs):
        start, _, end = _sibx_phases(refs[n:2 * n], *refs[2 * n:])
        start()
        end()

    return pl.pallas_call(
        body, name="rs_sibling_gather", out_shape=[jax.ShapeDtypeStruct(r.shape, r.dtype) for r in reds],
        in_specs=[ANY] * n, out_specs=[ANY] * n, input_output_aliases={u: u for u in range(n)},
        scratch_shapes=[pltpu.SemaphoreType.DMA((n,))] * 2)(*reds)


def _sibx_phases(o_refs, send_sems, recv_sems):
    n = len(o_refs)
    x, y, c = _place()
    cps = [pltpu.make_async_remote_copy(
        src_ref=o_refs[u].at[c], dst_ref=o_refs[u].at[c], send_sem=send_sems.at[u], recv_sem=recv_sems.at[u],
        device_id=(x, y, 1 - c), device_id_type=MESH) for u in range(n)]

    def start():
        for cp in cps:
            cp.start()

    def end():
        for u in range(n):
            cps[u].wait_send()
            pltpu.make_async_remote_copy(
                src_ref=o_refs[u].at[1 - c], dst_ref=o_refs[u].at[1 - c], send_sem=send_sems.at[u],
                recv_sem=recv_sems.at[u], device_id=(x, y, 1 - c), device_id_type=MESH).wait_recv()

    return start, None, end


def _carry_sibx(reds):
    n = len(reds)
    return _Carry(reds, [jax.ShapeDtypeStruct(r.shape, r.dtype) for r in reds], {u: u for u in range(n)},
                  [pltpu.SemaphoreType.DMA((n,))] * 2, lambda ins, outs, sems: _sibx_phases(outs, *sems))


def _ada_fwd(c16, w, b):
    n = w.shape[1]
    tn = 512

    def body(c_ref, w_ref, b_ref, o_ref):
        cc = c_ref[...]
        o_ref[...] = _dot(cc * _sig(cc), w_ref[...], prec=HI) + b_ref[...]

    return _pcall(body, name="ada_fwd", grid=(n // tn,),
                  in_specs=[_full((16, D)), pl.BlockSpec((D, tn), lambda j: (0, j)),
                            pl.BlockSpec((1, tn), lambda j: (0, j))],
                  out_specs=pl.BlockSpec((16, tn), lambda j: (0, j)),
                  out_shape=jax.ShapeDtypeStruct((16, n), F32))(c16, w, b)


def _ada_bwd(c16, dmod16, w):
    n = w.shape[1]
    tn = 512

    def body(c_ref, d_ref, w_ref, gw_ref, gc_ref):
        j = pl.program_id(0)

        @pl.when(j == 0)
        def _():
            gc_ref[...] = jnp.zeros_like(gc_ref)

        cc = c_ref[...]
        dm = d_ref[...]
        gw_ref[...] = _dot(cc * _sig(cc), dm, TN, prec=HI)
        gc_ref[...] += _dot(dm, w_ref[...], NT, prec=HI)

    return _pcall(body, name="ada_bwd", grid=(n // tn,),
                  in_specs=[_full((16, D)), pl.BlockSpec((16, tn), lambda j: (0, j)),
                            pl.BlockSpec((D, tn), lambda j: (0, j))],
                  out_specs=[pl.BlockSpec((D, tn), lambda j: (0, j)), _full((16, D))],
                  out_shape=[jax.ShapeDtypeStruct((D, n), F32),
                             jax.ShapeDtypeStruct((16, D), F32)])(c16, dmod16, w)


def _adam_math(w, g, m, v):
    c1 = 1.0 - ADAM_B1 ** ADAM_STEP
    c2 = 1.0 - ADAM_B2 ** ADAM_STEP
    nm = ADAM_B1 * m + (1.0 - ADAM_B1) * g
    nv = ADAM_B2 * v + (1.0 - ADAM_B2) * (g * g)
    return -ADAM_LR * ((nm / c1) / (jnp.sqrt(nv / c2) + ADAM_EPS) + ADAM_WD * w), nm, nv


def _adamw_small(ws, gs, ms, vs):
    n = len(ws)

    def body(*refs):
        for u in range(n):
            d_, nm, nv = _adam_math(refs[u][...], refs[n + u][...], refs[2 * n + u][...], refs[3 * n + u][...])
            refs[4 * n + u][...] = d_
            refs[5 * n + u][...] = nm
            refs[6 * n + u][...] = nv

    specs = [_full(w.shape) for w in ws]
    shapes = [jax.ShapeDtypeStruct(w.shape, F32) for w in ws]
    out = _pcall(body, name="adamw_small", grid=(1,), in_specs=specs * 4, out_specs=specs * 3,
                 out_shape=shapes * 3)(*ws, *gs, *ms, *vs)
    return out[:n], out[n:2 * n], out[2 * n:]


def _cctx_grad(parts, c_ctx):
    def body(p_ref, c_ref, o_ref):
        acc = p_ref[0:1, :]
        for k in range(1, 4):
            acc = acc + p_ref[k:k + 1, :]
        cc = c_ref[...]
        s = _sig(cc)
        o_ref[...] = acc * (s * (1.0 + cc * (1.0 - s)))

    return _pcall(body, name="cctx_grad", grid=(1,), in_specs=[_full(parts.shape), _full((1, D))],
                  out_specs=_full((1, D)), out_shape=jax.ShapeDtypeStruct((1, D), F32))(parts, c_ctx)


ADAM_STEPS = 8


def _adamw_multi(ws, gs, ms, vs, *, name, carry=None):
    n = len(ws)

    def body(*refs):
        for u in range(n):
            refs[4 * n + u][...], refs[5 * n + u][...], refs[6 * n + u][...] = _adam_math(
                refs[u][...], refs[n + u][...], refs[2 * n + u][...], refs[3 * n + u][...])

    specs = [pl.BlockSpec((w.shape[0] // ADAM_STEPS, w.shape[1]), lambda i: (i, 0)) for w in ws]
    shapes = [jax.ShapeDtypeStruct(w.shape, F32) for w in ws]
    res = _pcall(body, name=name, grid=(ADAM_STEPS,), in_specs=specs * 4, out_specs=specs * 3,
                 out_shape=shapes * 3, carry=carry)(*ws, *gs, *ms, *vs)
    out, extra = res if carry is not None else (res, None)
    return (out[:n], out[n:2 * n], out[2 * n:]), extra


def kernel(x, c, ctx, c_ctx, w_ada, b_ada, norm_mix_w, norm_ffn_w, w_in, hgrn_lb_logits, hgrn_norm_w, q_norm_w, k_norm_w, attn_sinks, w_branch_hgrn, w_branch_attn, w_out, w_ffn_gate, w_ffn_up, w_ffn_down, loss_target, m_c_ctx, m_w_ada, m_b_ada, m_norm_mix_w, m_norm_ffn_w, m_w_in, m_hgrn_lb_logits, m_hgrn_norm_w, m_q_norm_w, m_k_norm_w, m_attn_sinks, m_w_branch_hgrn, m_w_branch_attn, m_w_out, m_w_ffn_gate, m_w_ffn_up, m_w_ffn_down, v_c_ctx, v_w_ada, v_b_ada, v_norm_mix_w, v_norm_ffn_w, v_w_in, v_hgrn_lb_logits, v_hgrn_norm_w, v_q_norm_w, v_k_norm_w, v_attn_sinks, v_w_branch_hgrn, v_w_branch_attn, v_w_out, v_w_ffn_gate, v_w_ffn_up, v_w_ffn_down):
    xi, yi, ci = _place()
    chip = 2 * xi + yi
    dev = 2 * chip + ci
    s_len = x.shape[1]

    lbrow = jnp.pad(hgrn_lb_logits.reshape(1, 512), ((0, 0), (0, D - 512)))
    blk = jnp.concatenate([c, lbrow, jnp.zeros((6, D), F32)], axis=0)
    g0, = _allgather([blk], name="ag_cond", in_vmem=True)
    c16 = jnp.concatenate([g0[:, 0], c_ctx[None], jnp.zeros((7, D), F32)], axis=0)
    lg = g0[0::2, 1, :512].reshape(4, 2, 2, 128).transpose(1, 2, 0, 3).reshape(2, 2, HGW)

    nada = w_ada.shape[2]
    b_sh = lax.dynamic_slice(b_ada, (0, chip * nada), (1, nada))
    mod_sh = _ada_fwd(c16, w_ada[0], b_sh)
    g1, = _allgather([mod_sh], name="ag_mod", in_vmem=True)
    modall = g1[0::2].transpose(1, 0, 2).reshape(16, 4 * nada)
    mod = lax.dynamic_slice(modall, (dev, 0), (1, 6 * D)).reshape(6, D)
    modc = modall[8].reshape(6, D)[:2]

    shards = [w_in[0].T, w_branch_hgrn[0], w_branch_attn[0], w_out[0], w_ffn_gate[0].T, w_ffn_up[0].T,
              w_ffn_down[0]]
    bufs = _cast_place(shards, ci, dev)
    in8, = _allgather_inplace(bufs[0:1], name="ag_w_in")

    sq, gx, _, small, rs = _local_step(
        x[0], ctx[0], loss_target[0], mod, modc, norm_mix_w, norm_ffn_w, lg, hgrn_norm_w, q_norm_w,
        k_norm_w, attn_sinks[0], in8.reshape(NCOL, D), bufs[1:], dist=(ci, chip))
    loss = lax.psum(0.5 * jnp.sum(sq) / D, ("x", "y", "c"))

    def whole(r):
        return r.reshape(2 * r.shape[1], r.shape[2])

    g_dn, g_g, g_u = [whole(r) for r in rs["ffn_done"]]
    g_bh, g_ba, g_o = [whole(r) for r in rs["mix_done"]]
    in_pairs = _rs_pair_add(rs["in_units"], rs["in_recv"], ci)

    g2, tot = _ag_small(small["raw"])
    dmodc_tot = jnp.pad(tot[6:8].reshape(1, 2 * D), ((0, 0), (0, 4 * D)))
    g_b_ada = tot[0:6].reshape(1, 6 * D) + dmodc_tot
    dmod16 = jnp.concatenate([g2[:, 0:6].reshape(8, 6 * D), dmodc_tot, jnp.zeros((7, 6 * D), F32)], axis=0)
    g_w_ada, gc_part = _ada_bwd(c16, lax.dynamic_slice(dmod16, (0, chip * nada), (16, nada)), w_ada[0])
    g3, = _allgather([gc_part[8:16]], name="ag_cctx", in_vmem=True)
    g_c_ctx = _cctx_grad(g3[0::2, 0], c_ctx[None])[0]
    g_nw1 = tot[8:9]
    g_nw2 = tot[9:10]
    g_hw = tot[10, :HGW].reshape(4, HGD).sum(0, keepdims=True)
    g_qnw = tot[10, HGW:].reshape(8, HDIM).sum(0, keepdims=True)
    g_knw = tot[11, :128].reshape(2, HDIM).sum(0, keepdims=True)
    g_sinks = tot[16:24, 0][None]
    g_lg = lax.dynamic_slice(tot[12:16, :HGW].reshape(2, 2, HGW), (0, 0, chip * 128), (2, 2, 128))

    names = ["c_ctx", "w_ada", "b_ada", "norm_mix_w", "norm_ffn_w", "w_in", "hgrn_lb_logits", "hgrn_norm_w",
             "q_norm_w", "k_norm_w", "attn_sinks", "w_branch_hgrn", "w_branch_attn", "w_out", "w_ffn_gate",
             "w_ffn_up", "w_ffn_down"]
    ws = dict(zip(names, [c_ctx, w_ada, b_ada, norm_mix_w, norm_ffn_w, w_in, hgrn_lb_logits, hgrn_norm_w,
                          q_norm_w, k_norm_w, attn_sinks, w_branch_hgrn, w_branch_attn, w_out, w_ffn_gate,
                          w_ffn_up, w_ffn_down]))
    ms = dict(zip(names, [m_c_ctx, m_w_ada, m_b_ada, m_norm_mix_w, m_norm_ffn_w, m_w_in, m_hgrn_lb_logits,
                          m_hgrn_norm_w, m_q_norm_w, m_k_norm_w, m_attn_sinks, m_w_branch_hgrn,
                          m_w_branch_attn, m_w_out, m_w_ffn_gate, m_w_ffn_up, m_w_ffn_down]))
    vs = dict(zip(names, [v_c_ctx, v_w_ada, v_b_ada, v_norm_mix_w, v_norm_ffn_w, v_w_in, v_hgrn_lb_logits,
                          v_hgrn_norm_w, v_q_norm_w, v_k_norm_w, v_attn_sinks, v_w_branch_hgrn,
                          v_w_branch_attn, v_w_out, v_w_ffn_gate, v_w_ffn_up, v_w_ffn_down]))
    transposed = ("w_in", "w_ffn_gate", "w_ffn_up")

    def view(a, n):
        return a[0].T if n in transposed else a[0]

    def unview(a, n):
        return a.T[None] if n in transposed else a[None]

    delta, new_m, new_v, grads = {}, {}, {}, {}

    def big_adamw(group, gs, name, carry=None):
        (d_, m_, v_), extra = _adamw_multi([view(ws[n], n) for n in group], gs, [view(ms[n], n) for n in group],
                                           [view(vs[n], n) for n in group], name=name, carry=carry)
        for i, n in enumerate(group):
            grads[n], delta[n], new_m[n], new_v[n] = (unview(gs[i], n), unview(d_[i], n), unview(m_[i], n),
                                                      unview(v_[i], n))
        return extra

    in_contribs = big_adamw(["w_ffn_down", "w_ffn_gate", "w_ffn_up", "w_out", "w_branch_hgrn", "w_branch_attn"],
                            [g_dn, g_g, g_u, g_o, g_bh, g_ba], "adamw_first", carry=_carry_chipx(in_pairs))
    in_reds = _rs_chip_add(in_pairs, in_contribs, ci, chip)
    g_in, = [whole(r) for r in _rs_sibling_gather(in_reds)]
    big_adamw(["w_in", "w_ada"], [g_in, g_w_ada], "adamw_second")
    grads.update(c_ctx=g_c_ctx, b_ada=g_b_ada, norm_mix_w=g_nw1, norm_ffn_w=g_nw2, hgrn_lb_logits=g_lg,
                 hgrn_norm_w=g_hw, q_norm_w=g_qnw, k_norm_w=g_knw, attn_sinks=g_sinks)
    small_names = [n for n in names if n not in delta]

    def two_d(a):
        return a.reshape(1, -1) if a.ndim == 1 else a

    sd, sm_, sv = _adamw_small(*[[two_d(d[n]) for n in small_names] for d in (ws, grads, ms, vs)])
    for i, n in enumerate(small_names):
        for dst, src in ((delta, sd), (new_m, sm_), (new_v, sv)):
            dst[n] = src[i].reshape(ws[n].shape)
    return (loss, gx[None], *[grads[n] for n in names], *[delta[n] for n in names],
            *[new_m[n] for n in names], *[new_v[n] for n in names])
```

```python
import functools

import numpy as np
import jax
import jax.numpy as jnp
from jax import lax
from jax.experimental import pallas as pl
from jax.experimental.pallas import tpu as pltpu

F32 = jnp.float32
BF16 = jnp.bfloat16
HI = lax.Precision.HIGHEST
MESH = pl.DeviceIdType.MESH

D = 1024
L = 256
TM = 256
HGW = 512
HGD = 128
CH = 32
ATW = 512
HDIM = 64
BLK = 128
GRID_W = 64
DFF = 2816
NCOL = 5376
EPS = 1e-6
ROPE_THETA = 10000.0

C_FB, C_INP, C_QHG, C_FF = 0, 1, 2, 3
C_GATES = 1
C_GHG, C_QRAW = 8, 9
C_KV = 20
C_QKV = 6

ADAM_LR, ADAM_B1, ADAM_B2, ADAM_EPS, ADAM_WD, ADAM_STEP = 0.001, 0.9, 0.999, 1e-08, 0.01, 10

NN = (((1,), (0,)), ((), ()))
NT = (((1,), (1,)), ((), ()))
TN = (((0,), (0,)), ((), ()))


def _dot(a, b, dims=NN, prec=None):
    return lax.dot_general(a, b, dims, precision=prec, preferred_element_type=F32)


def _bdot(a, b, dims=NN):
    return _dot(a.astype(BF16), b.astype(BF16), dims)


def _sig(x):
    return 1.0 / (1.0 + jnp.exp(-x))


class _Carry:
    def __init__(self, ins, outs, aliases, scratch, phases):
        self.ins, self.outs, self.aliases, self.scratch, self.phases = ins, outs, aliases, scratch, phases


def _in_hbm(args):
    return [pltpu.with_memory_space_constraint(a, pltpu.HBM) for a in args]


def _carry_join(a, b):
    na_in, na_out, na_sc = len(a.ins), len(a.outs), len(a.scratch)
    aliases = dict(a.aliases)
    aliases.update({na_in + i: na_out + o for i, o in b.aliases.items()})

    def phases(ins, outs, sems):
        pa = a.phases(ins[:na_in], outs[:na_out], sems[:na_sc])
        pb = b.phases(ins[na_in:], outs[na_out:], sems[na_sc:])

        def both(fa, fb):
            if fa is None and fb is None:
                return None

            def run():
                for fn in (fa, fb):
                    if fn is not None:
                        fn()
            return run

        return tuple(both(fa, fb) for fa, fb in zip(pa, pb))

    return _Carry(list(a.ins) + list(b.ins), list(a.outs) + list(b.outs), aliases,
                  list(a.scratch) + list(b.scratch), phases)


def _pcall(body, *, name, grid, in_specs, out_specs, out_shape, scratch=(), aliases=None, vmem_mb=48,
           carry=None):
    params = pltpu.CompilerParams(dimension_semantics=("arbitrary",) * len(grid),
                                  vmem_limit_bytes=vmem_mb << 20)
    if carry is None:
        plain = pl.pallas_call(
            body, name=name, grid=grid, in_specs=in_specs, out_specs=out_specs, out_shape=out_shape,
            scratch_shapes=list(scratch), input_output_aliases=aliases or {}, compiler_params=params)
        return lambda *args: plain(*_in_hbm(args))
    single = not isinstance(out_shape, (list, tuple))
    out_specs_l = [out_specs] if single else list(out_specs)
    out_shape_l = [out_shape] if single else list(out_shape)
    n_in, n_out, n_sc = len(in_specs), len(out_shape_l), len(scratch)
    k_in, k_out = len(carry.ins), len(carry.outs)
    nsteps = int(np.prod(grid))
    assert nsteps >= 3

    def wrapped(*refs):
        ins, cins = refs[:n_in], refs[n_in:n_in + k_in]
        o0 = n_in + k_in
        outs, couts = refs[o0:o0 + n_out], refs[o0 + n_out:o0 + n_out + k_out]
        s0 = o0 + n_out + k_out
        sc, csc = refs[s0:s0 + n_sc], refs[s0 + n_sc:]
        step = pl.program_id(0)
        for ax in range(1, len(grid)):
            step = step * grid[ax] + pl.program_id(ax)
        start, mid, end = carry.phases(cins, couts, csc)
        pl.when(step == 0)(start)
        body(*ins, *outs, *sc)
        if mid is not None:
            pl.when(step == nsteps - 2)(mid)
        pl.when(step == nsteps - 1)(end)

    all_aliases = dict(aliases or {})
    all_aliases.update({n_in + i: n_out + o for i, o in carry.aliases.items()})
    call = pl.pallas_call(
        wrapped, name=name, grid=grid, in_specs=list(in_specs) + [ANY] * k_in,
        out_specs=out_specs_l + [ANY] * k_out, out_shape=out_shape_l + list(carry.outs),
        scratch_shapes=list(scratch) + list(carry.scratch), input_output_aliases=all_aliases,
        compiler_params=params)

    def run(*args):
        res = call(*_in_hbm(args), *carry.ins)
        core = res[:n_out]
        return (core[0] if single else list(core)), list(res[n_out:])

    return run


def _full(shape):
    nd = len(shape)
    return pl.BlockSpec(shape, lambda *_: (0,) * nd)


ANY = pl.BlockSpec(memory_space=pl.ANY)


def _mm(a, b, *, name, mode="nn", out_dtype=F32, tm, tn, tk):
    if mode == "nn":
        (m, k), (k2, n) = a.shape, b.shape
    elif mode == "nt":
        (m, k), (n, k2) = a.shape, b.shape
    else:
        (k, m), (k2, n) = a.shape, b.shape
    assert k == k2 and m % tm == 0 and n % tn == 0 and k % tk == 0, (name, a.shape, b.shape)
    nk = k // tk
    dims = {"nn": NN, "nt": NT, "tn": TN}[mode]

    def body(a_ref, b_ref, o_ref, acc):
        kk = pl.program_id(2)

        @pl.when(kk == 0)
        def _():
            acc[...] = jnp.zeros_like(acc)

        acc[...] += _bdot(a_ref[...], b_ref[...], dims)

        @pl.when(kk == nk - 1)
        def _():
            o_ref[...] = acc[...].astype(out_dtype)

    a_spec = (pl.BlockSpec((tk, tm), lambda i, j, kk: (kk, i)) if mode == "tn"
              else pl.BlockSpec((tm, tk), lambda i, j, kk: (i, kk)))
    b_spec = (pl.BlockSpec((tn, tk), lambda i, j, kk: (j, kk)) if mode == "nt"
              else pl.BlockSpec((tk, tn), lambda i, j, kk: (kk, j)))
    return _pcall(body, name=name, grid=(m // tm, n // tn, nk), in_specs=[a_spec, b_spec],
                  out_specs=pl.BlockSpec((tm, tn), lambda i, j, kk: (i, j)),
                  out_shape=jax.ShapeDtypeStruct((m, n), out_dtype),
                  scratch=[pltpu.VMEM((tm, tn), F32)])(a, b)


NT_IN = NCOL // 256


def _src_block(j):
    return j + jnp.where(j < 4, 2, jnp.where(j < 6, 3, jnp.where(j < 8, -6, jnp.where(
        j < 16, 5, jnp.where(j < 20, -7, -14)))))


def _mm_in(h, wt, tm, carry=None):
    tt = h.shape[0]

    def body(h_ref, w_ref, o_ref):
        o_ref[...] = _bdot(h_ref[...], w_ref[...], NT)

    return _pcall(body, name="mm_in", grid=(tt // tm, NT_IN),
                  in_specs=[pl.BlockSpec((tm, D), lambda i, j: (i, 0)),
                            pl.BlockSpec((256, D), lambda i, j: (_src_block(j), 0))],
                  out_specs=pl.BlockSpec((tm, 256), lambda i, j: (i, j)),
                  out_shape=jax.ShapeDtypeStruct((tt, NCOL), F32), carry=carry)(h, wt)


def _mm_dh(dp, wt, tm):
    tt = dp.shape[0]

    def body(d_ref, w_ref, o_ref, acc):
        kk = pl.program_id(1)

        @pl.when(kk == 0)
        def _():
            acc[...] = jnp.zeros_like(acc)

        acc[...] += _bdot(d_ref[...], w_ref[...])

        @pl.when(kk == NT_IN - 1)
        def _():
            o_ref[...] = acc[...]

    return _pcall(body, name="mm_dh", grid=(tt // tm, NT_IN),
                  in_specs=[pl.BlockSpec((tm, 256), lambda i, kk: (i, kk)),
                            pl.BlockSpec((256, D), lambda i, kk: (_src_block(kk), 0))],
                  out_specs=pl.BlockSpec((tm, D), lambda i, kk: (i, 0)),
                  out_shape=jax.ShapeDtypeStruct((tt, D), F32), scratch=[pltpu.VMEM((tm, D), F32)])(dp, wt)


def _mm_gin(dp, h, tk):
    tt = dp.shape[0]
    nk = tt // tk

    def body(d_ref, h_ref, o_ref, acc):
        kk = pl.program_id(1)

        @pl.when(kk == 0)
        def _():
            acc[...] = jnp.zeros_like(acc)

        acc[...] += _bdot(d_ref[...], h_ref[...], TN)

        @pl.when(kk == nk - 1)
        def _():
            o_ref[...] = acc[...]

    return _pcall(body, name="mm_gin", grid=(NT_IN, nk),
                  in_specs=[pl.BlockSpec((tk, 256), lambda j, kk: (kk, j)),
                            pl.BlockSpec((tk, D), lambda j, kk: (kk, 0))],
                  out_specs=pl.BlockSpec((256, D), lambda j, kk: (_src_block(j), 0)),
                  out_shape=jax.ShapeDtypeStruct((NCOL, D), F32), scratch=[pltpu.VMEM((256, D), F32)])(dp, h)


def _modulate(xin, nw, ss, *, name, sel):
    rows = xin.shape[0]

    def body(x_ref, nw_ref, ss_ref, h_ref):
        x = x_ref[...]
        r = lax.rsqrt(jnp.mean(x * x, axis=-1, keepdims=True) + EPS)
        s = ss_ref[0]
        h_ref[...] = ((x * r * nw_ref[...]) * (1.0 + s[1:2]) + s[0:1]).astype(BF16)

    return _pcall(body, name=name, grid=(rows // TM,),
                  in_specs=[pl.BlockSpec((TM, D), lambda i: (i, 0)), _full((1, D)),
                            pl.BlockSpec((1, 2, D), lambda i: (sel(i), 0, 0))],
                  out_specs=pl.BlockSpec((TM, D), lambda i: (i, 0)),
                  out_shape=jax.ShapeDtypeStruct((rows, D), BF16))(xin, nw, ss)


def _norm_bwd_rows(x, dh, nw, scale):
    r = lax.rsqrt(jnp.mean(x * x, axis=-1, keepdims=True) + EPS)
    xh = x * r
    dxh = dh * ((1.0 + scale) * nw)
    dx = r * (dxh - xh * jnp.mean(dxh * xh, axis=-1, keepdims=True))
    return dx, xh


def _res1_mod2(x, ao, g1, nw2, ss2):
    s_len = x.shape[0]

    def body(x_ref, ao_ref, g_ref, nw_ref, ss_ref, x1_ref, h_ref):
        x1 = x_ref[...] + g_ref[...] * ao_ref[...]
        x1_ref[...] = x1
        r = lax.rsqrt(jnp.mean(x1 * x1, axis=-1, keepdims=True) + EPS)
        s = ss_ref[0]
        h_ref[...] = ((x1 * r * nw_ref[...]) * (1.0 + s[1:2]) + s[0:1]).astype(BF16)

    row = pl.BlockSpec((TM, D), lambda i: (i, 0))
    return _pcall(body, name="res1_mod2", grid=(s_len // TM,),
                  in_specs=[row, row, _full((1, D)), _full((1, D)), _full((1, 2, D))],
                  out_specs=[row, row],
                  out_shape=[jax.ShapeDtypeStruct((s_len, D), F32),
                             jax.ShapeDtypeStruct((s_len, D), BF16)])(x, ao, g1, nw2, ss2)


TS = 512


def _acc_call(body, *, name, grid, in_specs, out_specs, out_shape, acc_shapes, args):
    return _pcall(body, name=name, grid=grid, in_specs=in_specs, out_specs=out_specs, out_shape=out_shape,
                  scratch=[pltpu.VMEM(s, F32) for s in acc_shapes])(*args)


def _mm_cs(a, w4, *, name):
    m, k = a.shape
    _, _, ns = w4.shape

    def body(a_ref, w_ref, o_ref):
        o_ref[...] = _bdot(a_ref[...], w_ref[0])

    return _pcall(body, name=name, grid=(m // TS, 4),
                  in_specs=[pl.BlockSpec((TS, k), lambda i, j: (i, 0)),
                            pl.BlockSpec((1, k, ns), lambda i, j: (j, 0, 0))],
                  out_specs=pl.BlockSpec((TS, ns), lambda i, j: (i, j)),
                  out_shape=jax.ShapeDtypeStruct((m, 4 * ns), F32))(a, w4)


def _mm_cs_nt(a, w4, *, name):
    m = a.shape[0]
    _, k, ns = w4.shape

    def body(a_ref, w_ref, o_ref, acc):
        j = pl.program_id(1)

        @pl.when(j == 0)
        def _():
            acc[...] = jnp.zeros_like(acc)

        acc[...] += _bdot(a_ref[...], w_ref[0], NT)

        @pl.when(j == 3)
        def _():
            o_ref[...] = acc[...]

    return _acc_call(body, name=name, grid=(m // TS, 4),
                     in_specs=[pl.BlockSpec((TS, ns), lambda i, j: (i, j)),
                               pl.BlockSpec((1, k, ns), lambda i, j: (j, 0, 0))],
                     out_specs=pl.BlockSpec((TS, k), lambda i, j: (i, 0)),
                     out_shape=jax.ShapeDtypeStruct((m, k), F32), acc_shapes=[(TS, k)], args=(a, w4))


def _mm_cs_tn(a, b, ns, *, name):
    s_len, k = a.shape
    nk = s_len // TS

    def body(a_ref, b_ref, o_ref, acc):
        t = pl.program_id(1)

        @pl.when(t == 0)
        def _():
            acc[...] = jnp.zeros_like(acc)

        acc[...] += _bdot(a_ref[...], b_ref[...], TN)

        @pl.when(t == nk - 1)
        def _():
            o_ref[0] = acc[...]

    return _acc_call(body, name=name, grid=(4, nk),
                     in_specs=[pl.BlockSpec((TS, k), lambda j, t: (t, 0)),
                               pl.BlockSpec((TS, ns), lambda j, t: (t, j))],
                     out_specs=pl.BlockSpec((1, k, ns), lambda j, t: (j, 0, 0)),
                     out_shape=jax.ShapeDtypeStruct((4, k, ns), F32), acc_shapes=[(k, ns)], args=(a, b))


def _ffn_up(h2, g4, u4):
    s_len = h2.shape[0]
    ns = g4.shape[1]

    def body(h_ref, g_ref, u_ref, a_ref, b_ref, z_ref):
        h = h_ref[...]
        a = _bdot(h, g_ref[0], NT)
        b = _bdot(h, u_ref[0], NT)
        a_ref[0] = a
        b_ref[0] = b
        z_ref[0] = (a * _sig(a) * b).astype(BF16)

    w = pl.BlockSpec((1, ns, D), lambda i, j: (j, 0, 0))
    o = pl.BlockSpec((1, TS, ns), lambda i, j: (j, i, 0))
    f = jax.ShapeDtypeStruct((4, s_len, ns), F32)
    return _pcall(body, name="ffn_up", grid=(s_len // TS, 4),
                  in_specs=[pl.BlockSpec((TS, D), lambda i, j: (i, 0)), w, w], out_specs=[o, o, o],
                  out_shape=[f, f, jax.ShapeDtypeStruct((4, s_len, ns), BF16)])(h2, g4, u4)


def _ffn_down(z4, dn4):
    _, s_len, ns = z4.shape

    def body(z_ref, w_ref, o_ref, acc):
        j = pl.program_id(1)

        @pl.when(j == 0)
        def _():
            acc[...] = jnp.zeros_like(acc)

        acc[...] += _bdot(z_ref[0], w_ref[0])

        @pl.when(j == 3)
        def _():
            o_ref[...] = acc[...]

    return _acc_call(body, name="ffn_down", grid=(s_len // TS, 4),
                     in_specs=[pl.BlockSpec((1, TS, ns), lambda i, j: (j, i, 0)),
                               pl.BlockSpec((1, ns, D), lambda i, j: (j, 0, 0))],
                     out_specs=pl.BlockSpec((TS, D), lambda i, j: (i, 0)),
                     out_shape=jax.ShapeDtypeStruct((s_len, D), F32), acc_shapes=[(TS, D)], args=(z4, dn4))


def _ffn_dz(dyb, dn4, a4, b4):
    _, s_len, ns = a4.shape

    def body(dy_ref, w_ref, a_ref, b_ref, da_ref, db_ref):
        dz = _bdot(dy_ref[...], w_ref[0], NT)
        a = a_ref[0]
        s = _sig(a)
        da_ref[0] = (dz * b_ref[0] * (s * (1.0 + a * (1.0 - s)))).astype(BF16)
        db_ref[0] = (dz * (a * s)).astype(BF16)

    t = pl.BlockSpec((1, TS, ns), lambda i, j: (j, i, 0))
    o = jax.ShapeDtypeStruct((4, s_len, ns), BF16)
    return _pcall(body, name="ffn_dz", grid=(s_len // TS, 4),
                  in_specs=[pl.BlockSpec((TS, D), lambda i, j: (i, 0)),
                            pl.BlockSpec((1, ns, D), lambda i, j: (j, 0, 0)), t, t],
                  out_specs=[t, t], out_shape=[o, o])(dyb, dn4, a4, b4)


def _ffn_gdn(z4, dyb):
    _, s_len, ns = z4.shape
    nk = s_len // TS

    def body(z_ref, dy_ref, o_ref, acc):
        t = pl.program_id(1)

        @pl.when(t == 0)
        def _():
            acc[...] = jnp.zeros_like(acc)

        acc[...] += _bdot(z_ref[0], dy_ref[...], TN)

        @pl.when(t == nk - 1)
        def _():
            o_ref[0] = acc[...]

    return _acc_call(body, name="ffn_gdn", grid=(4, nk),
                     in_specs=[pl.BlockSpec((1, TS, ns), lambda j, t: (j, t, 0)),
                               pl.BlockSpec((TS, D), lambda j, t: (t, 0))],
                     out_specs=pl.BlockSpec((1, ns, D), lambda j, t: (j, 0, 0)),
                     out_shape=jax.ShapeDtypeStruct((4, ns, D), F32), acc_shapes=[(ns, D)], args=(z4, dyb))


def _ffn_dh2(da4, db4, g4, u4):
    _, s_len, ns = da4.shape

    def body(da_ref, db_ref, g_ref, u_ref, o_ref, acc):
        j = pl.program_id(1)

        @pl.when(j == 0)
        def _():
            acc[...] = jnp.zeros_like(acc)

        acc[...] += _bdot(da_ref[0], g_ref[0]) + _bdot(db_ref[0], u_ref[0])

        @pl.when(j == 3)
        def _():
            o_ref[...] = acc[...]

    t = pl.BlockSpec((1, TS, ns), lambda i, j: (j, i, 0))
    w = pl.BlockSpec((1, ns, D), lambda i, j: (j, 0, 0))
    return _acc_call(body, name="ffn_dh2", grid=(s_len // TS, 4), in_specs=[t, t, w, w],
                     out_specs=pl.BlockSpec((TS, D), lambda i, j: (i, 0)),
                     out_shape=jax.ShapeDtypeStruct((s_len, D), F32), acc_shapes=[(TS, D)],
                     args=(da4, db4, g4, u4))


def _ffn_ggu(h2, da4, db4):
    _, s_len, ns = da4.shape
    nk = s_len // TS

    def body(h_ref, da_ref, db_ref, gg_ref, gu_ref, acc_g, acc_u):
        t = pl.program_id(1)

        @pl.when(t == 0)
        def _():
            acc_g[...] = jnp.zeros_like(acc_g)
            acc_u[...] = jnp.zeros_like(acc_u)

        h = h_ref[...]
        acc_g[...] += _bdot(da_ref[0], h, TN)
        acc_u[...] += _bdot(db_ref[0], h, TN)

        @pl.when(t == nk - 1)
        def _():
            gg_ref[0] = acc_g[...]
            gu_ref[0] = acc_u[...]

    d = pl.BlockSpec((1, TS, ns), lambda j, t: (j, t, 0))
    o = pl.BlockSpec((1, ns, D), lambda j, t: (j, 0, 0))
    f = jax.ShapeDtypeStruct((4, ns, D), F32)
    return _acc_call(body, name="ffn_ggu", grid=(4, nk),
                     in_specs=[pl.BlockSpec((TS, D), lambda j, t: (t, 0)), d, d], out_specs=[o, o],
                     out_shape=[f, f], acc_shapes=[(ns, D), (ns, D)], args=(h2, da4, db4))


def _loss_head(x1, y, g2, tgt):
    s_len = x1.shape[0]

    def body(x1_ref, y_ref, g_ref, t_ref, sq_ref, dx2_ref, dyb_ref, dg_ref):
        i = pl.program_id(0)

        @pl.when(i == 0)
        def _():
            sq_ref[...] = jnp.zeros_like(sq_ref)
            dg_ref[...] = jnp.zeros_like(dg_ref)

        y_ = y_ref[...]
        g = g_ref[...]
        e = x1_ref[...] + g * y_ - t_ref[...]
        sq_ref[...] += jnp.sum(e * e, axis=0, keepdims=True)
        dx2 = e * (1.0 / D)
        dx2_ref[...] = dx2
        dyb_ref[...] = (g * dx2).astype(BF16)
        dg_ref[...] += jnp.sum(dx2 * y_, axis=0, keepdims=True)

    row = pl.BlockSpec((TM, D), lambda i: (i, 0))
    vec = _full((1, D))
    return _pcall(body, name="loss_head", grid=(s_len // TM,),
                  in_specs=[row, row, vec, row], out_specs=[vec, row, row, vec],
                  out_shape=[jax.ShapeDtypeStruct((1, D), F32), jax.ShapeDtypeStruct((s_len, D), F32),
                             jax.ShapeDtypeStruct((s_len, D), BF16),
                             jax.ShapeDtypeStruct((1, D), F32)])(x1, y, g2, tgt)


def _mod2_bwd(x1, dh2, dx2, ao, nw2, ss2, g1):
    s_len = x1.shape[0]

    def body(x1_ref, dh_ref, dx2_ref, ao_ref, nw_ref, ss_ref, g_ref,
             dx1_ref, da_ref, dss_ref, dnw_ref, dg_ref):
        i = pl.program_id(0)

        @pl.when(i == 0)
        def _():
            dss_ref[...] = jnp.zeros_like(dss_ref)
            dnw_ref[...] = jnp.zeros_like(dnw_ref)
            dg_ref[...] = jnp.zeros_like(dg_ref)

        dh = dh_ref[...]
        nw = nw_ref[...]
        scale = ss_ref[0][1:2]
        dxn, xh = _norm_bwd_rows(x1_ref[...], dh, nw, scale)
        dx1 = dx2_ref[...] + dxn
        dx1_ref[...] = dx1
        da_ref[...] = (g_ref[...] * dx1).astype(BF16)
        dg_ref[...] += jnp.sum(dx1 * ao_ref[...], axis=0, keepdims=True)
        dsh = jnp.sum(dh, axis=0, keepdims=True)
        dsc = jnp.sum(dh * xh * nw, axis=0, keepdims=True)
        dss_ref[...] += jnp.concatenate([dsh, dsc], axis=0)
        dnw_ref[...] += jnp.sum(dh * xh * (1.0 + scale), axis=0, keepdims=True)

    row = pl.BlockSpec((TM, D), lambda i: (i, 0))
    vec = _full((1, D))
    return _pcall(body, name="mod2_bwd", grid=(s_len // TM,),
                  in_specs=[row, row, row, row, vec, _full((1, 2, D)), vec],
                  out_specs=[row, row, _full((2, D)), vec, vec],
                  out_shape=[jax.ShapeDtypeStruct((s_len, D), F32), jax.ShapeDtypeStruct((s_len, D), BF16),
                             jax.ShapeDtypeStruct((2, D), F32), jax.ShapeDtypeStruct((1, D), F32),
                             jax.ShapeDtypeStruct((1, D), F32)])(x1, dh2, dx2, ao, nw2, ss2, g1)


def _mod1_bwd(tok, dh, dx1, nw1, ss1, carry=None):
    tt = tok.shape[0]
    s_len = dx1.shape[0]

    def body(t_ref, dh_ref, dx1_ref, nw_ref, ss_ref, dx_ref, dss_ref, dnw_ref):
        i = pl.program_id(0)

        @pl.when(i == 0)
        def _():
            dnw_ref[...] = jnp.zeros_like(dnw_ref)

        @pl.when(i <= 1)
        def _():
            dss_ref[...] = jnp.zeros_like(dss_ref)

        dh_ = dh_ref[...]
        nw = nw_ref[...]
        scale = ss_ref[0][1:2]
        dxn, xh = _norm_bwd_rows(t_ref[...], dh_, nw, scale)

        @pl.when(i >= 1)
        def _():
            dx_ref[...] = dx1_ref[...] + dxn

        dsh = jnp.sum(dh_, axis=0, keepdims=True)
        dsc = jnp.sum(dh_ * xh * nw, axis=0, keepdims=True)
        dss_ref[...] += jnp.concatenate([dsh, dsc], axis=0)[None]
        dnw_ref[...] += jnp.sum(dh_ * xh * (1.0 + scale), axis=0, keepdims=True)

    row = pl.BlockSpec((TM, D), lambda i: (i, 0))
    lat = pl.BlockSpec((TM, D), lambda i: (jnp.maximum(i - 1, 0), 0))
    sel = pl.BlockSpec((1, 2, D), lambda i: (jnp.minimum(i, 1), 0, 0))
    return _pcall(body, name="mod1_bwd", grid=(tt // TM,),
                  in_specs=[row, row, lat, _full((1, D)), sel],
                  out_specs=[lat, sel, _full((1, D))],
                  out_shape=[jax.ShapeDtypeStruct((s_len, D), F32), jax.ShapeDtypeStruct((2, 2, D), F32),
                             jax.ShapeDtypeStruct((1, D), F32)], carry=carry)(tok, dh, dx1, nw1, ss1)


def _rows(c):
    return slice(c * CH, (c + 1) * CH)


def _chunk_masks(rev, transpose=False):
    r = lax.broadcasted_iota(jnp.int32, (TM, TM), 0)
    c = lax.broadcasted_iota(jnp.int32, (TM, TM), 1)
    same = (r // CH) == (c // CH)
    before = (c >= r) if (rev != transpose) else (c <= r)
    return same & before, same


def _hgrn_gate(fl, qraw, lg):
    lb = 1.0 / (1.0 + jnp.exp(lg[1:2] - lg[0:1]))
    sg = _sig(fl)
    f = lb + (1.0 - lb) * sg
    q = qraw * _sig(qraw) * (HGD ** -0.5)
    return lb, sg, f, q


def _hgrn_fwd(p, lg, *, rev, carry=None):
    tt = p.shape[0]
    nt = tt // TM
    ncht = TM // CH
    d = 1 if rev else 0

    def tile_of(s):
        return jnp.where(s == 0, 0, nt - s) if rev else s

    def body(f_ref, inp_ref, q_ref, lg_ref, o_ref, st_ref, state):
        s = pl.program_id(0)

        @pl.when(s == 0)
        def _():
            state[...] = jnp.zeros_like(state)

        _, _, f, q = _hgrn_gate(f_ref[...], q_ref[...], lg_ref[0])
        lf = jnp.log(f)
        causal, same = _chunk_masks(rev)
        cum = _dot(causal.astype(F32), lf, prec=HI)
        tot = _dot(same.astype(F32), lf, prec=HI)
        qd = (q * jnp.exp(cum)).astype(BF16)
        kd = ((1.0 - f) * jnp.exp(-cum)).astype(BF16)
        ke = ((1.0 - f) * jnp.exp(tot - cum)).astype(BF16)
        et = jnp.exp(tot)
        v = inp_ref[...].astype(BF16)
        order = range(ncht - 1, -1, -1) if rev else range(ncht)
        outs = []
        for h in range(4):
            sl = slice(h * HGD, (h + 1) * HGD)
            qd_, kd_, ke_, v_ = qd[:, sl], kd[:, sl], ke[:, sl], v[:, sl]
            pm = jnp.where(causal, _dot(qd_, kd_, NT), 0.0).astype(BF16)
            o_h = _dot(pm, v_)
            upd = [_dot(v_[_rows(c)], ke_[_rows(c)], TN) for c in range(ncht)]
            st = state[h]
            for c in order:
                st_ref[c, h] = st
                st = st * et[c * CH:c * CH + 1, sl] + upd[c]
            state[h] = st
            inter = [_dot(qd_[_rows(c)], st_ref[c, h].astype(BF16), NT) for c in range(ncht)]
            outs.append(o_h + jnp.concatenate(inter, axis=0))
        o_ref[...] = jnp.concatenate(outs, axis=1)

    def col(cb):
        return pl.BlockSpec((TM, HGW), lambda s: (tile_of(s), cb))

    return _pcall(
        body, name="hgrn_fwd_rev" if rev else "hgrn_fwd", grid=(nt,),
        in_specs=[col(C_FB if rev else C_FF), col(C_INP), col(C_QHG),
                  pl.BlockSpec((1, 2, HGW), lambda s: (d, 0, 0))],
        out_specs=[pl.BlockSpec((TM, HGW), lambda s: (tile_of(s), 0)),
                   pl.BlockSpec((ncht, 4, HGD, HGD), lambda s: (tile_of(s), 0, 0, 0))],
        out_shape=[jax.ShapeDtypeStruct((tt, HGW), F32),
                   jax.ShapeDtypeStruct((nt * ncht, 4, HGD, HGD), F32)],
        scratch=[pltpu.VMEM((4, HGD, HGD), F32)], carry=carry)(p, p, p, lg)


def _hgrn_bwd(p, lg, do, st, dp, prev, *, rev, carry=None):
    tt = p.shape[0]
    nt = tt // TM
    ncht = TM // CH
    d = 1 if rev else 0
    second = prev is not None

    def tile_of(s):
        return jnp.where(s == nt - 1, 0, s + 1) if rev else nt - 1 - s

    def body(*refs):
        if second:
            (f_ref, inp_ref, q_ref, lg_ref, do_ref, st_ref, dvp_ref, dqp_ref, _dp_in,
             dp_ref, dlg_ref, dstate) = refs
        else:
            (f_ref, inp_ref, q_ref, lg_ref, do_ref, st_ref, _dp_in,
             dp_ref, dv_ref, dq_ref, dlg_ref, dstate) = refs
        s = pl.program_id(0)
        tile = tile_of(s)

        @pl.when(s == 0)
        def _():
            dstate[...] = jnp.zeros_like(dstate)
            dlg_ref[...] = jnp.zeros_like(dlg_ref)

        qraw = q_ref[...]
        lb, sg, f, q = _hgrn_gate(f_ref[...], qraw, lg_ref[0])
        lf = jnp.log(f)
        causal, same = _chunk_masks(rev)
        causal_t, _ = _chunk_masks(rev, transpose=True)
        cum = _dot(causal.astype(F32), lf, prec=HI)
        tot = _dot(same.astype(F32), lf, prec=HI)
        ea, eb, ee, et = jnp.exp(cum), jnp.exp(-cum), jnp.exp(tot - cum), jnp.exp(tot)
        qdf, kdf, kef = q * ea, (1.0 - f) * eb, (1.0 - f) * ee
        qd, kd, ke = qdf.astype(BF16), kdf.astype(BF16), kef.astype(BF16)
        v = inp_ref[...].astype(BF16)
        dob = jnp.where(tile == 0, 0.0, do_ref[...]).astype(BF16)
        order = range(ncht) if rev else range(ncht - 1, -1, -1)
        dq_l, dk_l, dv_l, dcum_l, dtot_l = [], [], [], [], []
        for h in range(4):
            sl = slice(h * HGD, (h + 1) * HGD)
            qd_, kd_, ke_, v_, do_ = qd[:, sl], kd[:, sl], ke[:, sl], v[:, sl], dob[:, sl]
            pmt = jnp.where(causal_t, _dot(kd_, qd_, NT), 0.0).astype(BF16)
            dpm = jnp.where(causal, _dot(do_, v_, NT), 0.0).astype(BF16)
            dpmt = jnp.where(causal_t, _dot(v_, do_, NT), 0.0).astype(BF16)
            dv = _dot(pmt, do_)
            dqd = _dot(dpm, kd_)
            dkd = _dot(dpmt, qd_)
            upd = [_dot(do_[_rows(c)], qd_[_rows(c)], TN) for c in range(ncht)]
            ds = dstate[h]
            ds1 = [None] * ncht
            for c in order:
                ds1[c] = ds
                ds = ds * et[c * CH:c * CH + 1, sl] + upd[c]
            dstate[h] = ds
            dke_c, dv_c, dqd_c, dtot_c = [], [], [], []
            for c in range(ncht):
                st0 = st_ref[c, h]
                dsb = ds1[c].astype(BF16)
                dke_ = _dot(v_[_rows(c)], dsb)
                dke_c.append(dke_)
                dv_c.append(_dot(ke_[_rows(c)], dsb, NT))
                dqd_c.append(_dot(do_[_rows(c)], st0.astype(BF16)))
                dt = (jnp.sum(ds1[c] * st0, axis=0, keepdims=True) * et[c * CH:c * CH + 1, sl]
                      + jnp.sum(dke_ * kef[_rows(c), sl], axis=0, keepdims=True))
                dtot_c.append(jnp.broadcast_to(dt, (CH, HGD)))
            dke = jnp.concatenate(dke_c, axis=0)
            dqd = dqd + jnp.concatenate(dqd_c, axis=0)
            dv_l.append(dv + jnp.concatenate(dv_c, axis=0))
            dtot_l.append(jnp.concatenate(dtot_c, axis=0))
            dq_l.append(dqd * ea[:, sl])
            dk_l.append(dkd * eb[:, sl] + dke * ee[:, sl])
            dcum_l.append(dqd * qdf[:, sl] - dkd * kdf[:, sl] - dke * kef[:, sl])
        dcum = jnp.concatenate(dcum_l, axis=1)
        dlf = _dot(causal_t.astype(F32), dcum, prec=HI) + jnp.concatenate(dtot_l, axis=1)
        dq_t = jnp.concatenate(dq_l, axis=1)
        dv_t = jnp.concatenate(dv_l, axis=1)

        df = dlf / f - jnp.concatenate(dk_l, axis=1)
        dfl = df * (1.0 - lb) * sg * (1.0 - sg)
        dlb = jnp.sum(df * (1.0 - sg), axis=0, keepdims=True)
        dl0 = dlb * lb * (1.0 - lb)
        dlg_ref[...] += jnp.concatenate([dl0, -dl0], axis=0)[None]
        if second:
            sq = _sig(qraw)
            dqr = (dqp_ref[...] + dq_t) * (HGD ** -0.5) * (sq * (1.0 + qraw * (1.0 - sq)))
            dp_ref[...] = jnp.concatenate([dfl, dvp_ref[...] + dv_t, dqr], axis=1).astype(BF16)
        else:
            dp_ref[...] = dfl.astype(BF16)
            dv_ref[...] = dv_t
            dq_ref[...] = dq_t

    def col(cb):
        return pl.BlockSpec((TM, HGW), lambda s: (tile_of(s), cb))

    tok = pl.BlockSpec((TM, HGW), lambda s: (tile_of(s), 0))
    in_specs = [col(C_FB if rev else C_FF), col(C_INP), col(C_QHG),
                pl.BlockSpec((1, 2, HGW), lambda s: (d, 0, 0)),
                pl.BlockSpec((TM, HGW), lambda s: (jnp.maximum(tile_of(s) - 1, 0), 0)),
                pl.BlockSpec((ncht, 4, HGD, HGD), lambda s: (tile_of(s), 0, 0, 0))]
    args = [p, p, p, lg, do, st]
    dlg_spec = _full((1, 2, HGW))
    dlg_shape = jax.ShapeDtypeStruct((1, 2, HGW), F32)
    if second:
        in_specs += [tok, tok]
        args += [prev[0], prev[1]]
        out_specs = [pl.BlockSpec((TM, 3 * HGW), lambda s: (tile_of(s), 0)), dlg_spec]
        out_shape = [jax.ShapeDtypeStruct(dp.shape, BF16), dlg_shape]
    else:
        out_specs = [pl.BlockSpec((TM, HGW), lambda s: (tile_of(s), C_FB if rev else C_FF)), tok, tok, dlg_spec]
        out_shape = [jax.ShapeDtypeStruct(dp.shape, BF16), jax.ShapeDtypeStruct((tt, HGW), F32),
                     jax.ShapeDtypeStruct((tt, HGW), F32), dlg_shape]
    in_specs.append(ANY)
    args.append(dp)
    return _pcall(body, name="hgrn_bwd_rev" if rev else "hgrn_bwd", grid=(nt,),
                  in_specs=in_specs, out_specs=out_specs, out_shape=out_shape,
                  scratch=[pltpu.VMEM((4, HGD, HGD), F32)],
                  aliases={len(args) - 1: 0}, carry=carry)(*args)


def _head_rms(o, w, nheads):
    outs = []
    for h in range(nheads):
        oh = o[:, h * HGD:(h + 1) * HGD]
        outs.append(oh * lax.rsqrt(jnp.mean(oh * oh, axis=-1, keepdims=True) + EPS))
    return jnp.concatenate(outs, axis=1)


def _readout(o0, o1, p, hw4):
    s_len = o0.shape[0] - L

    def body(o0_ref, o1_ref, g_ref, w_ref, y_ref):
        xh = _head_rms(o0_ref[...] + o1_ref[...], None, 4)
        g = g_ref[...]
        y_ref[...] = (xh * w_ref[...] * (g * _sig(g))).astype(BF16)

    lat = pl.BlockSpec((TM, HGW), lambda i: (i + 1, 0))
    return _pcall(body, name="readout", grid=(s_len // TM,),
                  in_specs=[lat, lat, pl.BlockSpec((TM, HGW), lambda i: (i + 1, C_GHG)), _full((1, HGW))],
                  out_specs=pl.BlockSpec((TM, HGW), lambda i: (i, 0)),
                  out_shape=jax.ShapeDtypeStruct((s_len, HGW), BF16))(o0, o1, p, hw4)


def _readout_bwd(o0, o1, p, hw4, dy, dp, carry=None):
    tt = o0.shape[0]
    s_len = tt - L

    def body(o0_ref, o1_ref, g_ref, w_ref, dy_ref, _dp_in, dp_ref, do_ref, dw_ref):
        i = pl.program_id(0)

        @pl.when(i == 0)
        def _():
            dw_ref[...] = jnp.zeros_like(dw_ref)
            dp_ref[...] = jnp.zeros_like(dp_ref)

        @pl.when(i >= 1)
        def _():
            o = o0_ref[...] + o1_ref[...]
            g = g_ref[...]
            w = w_ref[...]
            sg = _sig(g)
            dy_ = dy_ref[...]
            dsw = dy_ * (g * sg)
            outs, xhs = [], []
            for h in range(4):
                sl = slice(h * HGD, (h + 1) * HGD)
                oh = o[:, sl]
                r = lax.rsqrt(jnp.mean(oh * oh, axis=-1, keepdims=True) + EPS)
                xh = oh * r
                dxh = dsw[:, sl] * w[:, sl]
                outs.append(r * (dxh - xh * jnp.mean(dxh * xh, axis=-1, keepdims=True)))
                xhs.append(xh)
            xh = jnp.concatenate(xhs, axis=1)
            do_ref[...] = jnp.concatenate(outs, axis=1)
            dp_ref[...] = (dy_ * xh * w * (sg * (1.0 + g * (1.0 - sg)))).astype(BF16)
            dw_ref[...] += jnp.sum(dsw * xh, axis=0, keepdims=True)

    tok = pl.BlockSpec((TM, HGW), lambda i: (i, 0))
    lat = pl.BlockSpec((TM, HGW), lambda i: (jnp.maximum(i - 1, 0), 0))
    return _pcall(body, name="readout_bwd", grid=(tt // TM,),
                  in_specs=[tok, tok, pl.BlockSpec((TM, HGW), lambda i: (i, C_GHG)), _full((1, HGW)), lat, ANY],
                  out_specs=[pl.BlockSpec((TM, HGW), lambda i: (i, C_GHG)), lat, _full((1, HGW))],
                  out_shape=[jax.ShapeDtypeStruct(dp.shape, BF16), jax.ShapeDtypeStruct((s_len, HGW), F32),
                             jax.ShapeDtypeStruct((1, HGW), F32)],
                  aliases={5: 0}, carry=carry)(o0, o1, p, hw4, dy, dp)


def _rope_tables(s_len):
    t = np.arange(s_len)
    inv = ROPE_THETA ** (-np.arange(0, 32, 2, dtype=np.float64) / 32)
    def half(pos):
        ang = pos[:, None].astype(np.float64) * inv[None, :]
        return (np.concatenate([np.cos(ang), np.cos(ang)], 1), np.concatenate([-np.sin(ang), np.sin(ang)], 1))
    cr, sr = half(t // GRID_W)
    cc, sc = half(t % GRID_W)
    cos = np.concatenate([cr, cc, cr, cc], 1)
    sin = np.concatenate([sr, sc, sr, sc], 1)
    cos = np.concatenate([np.ones((L, 128)), cos], 0)
    sin = np.concatenate([np.zeros((L, 128)), sin], 0)
    return jnp.asarray(cos, F32), jnp.asarray(sin, F32)


def _blockdiag(n, w):
    i = np.arange(n)
    return jnp.asarray((i[:, None] // w == i[None, :] // w) / float(w), F32)


def _dup_matrix():
    m = np.zeros((128, 512), np.float32)
    for g in range(2):
        for j in range(4):
            for dd in range(HDIM):
                m[64 * g + dd, 256 * g + 64 * j + dd] = 1.0
    return m


def _rot(x):
    n = x.shape[1]
    lane = lax.broadcasted_iota(jnp.int32, x.shape, 1)
    return jnp.where((lane % 32) < 16, pltpu.roll(x, n - 16, 1), pltpu.roll(x, 16, 1))


def _qk_prep(p, cos, sin, qnw8, knw2, bd512, bd128, dup):
    tt = p.shape[0]

    def body(q_ref, kv_ref, cos_ref, sin_ref, qw_ref, kw_ref, b5_ref, b1_ref, dup_ref,
             qr_ref, k4_ref, v4_ref):
        cos_, sin_ = cos_ref[...], sin_ref[...]
        q = q_ref[...]
        qn = q * lax.rsqrt(_dot(q * q, b5_ref[...], prec=HI) + EPS) * qw_ref[...]
        cos4 = jnp.concatenate([cos_] * 4, axis=1)
        sin4 = jnp.concatenate([sin_] * 4, axis=1)
        qr_ref[...] = ((qn * cos4 + _rot(qn) * sin4) * (HDIM ** -0.5)).astype(BF16)
        kv = kv_ref[...]
        k, v = kv[:, :128], kv[:, 128:]
        kn = k * lax.rsqrt(_dot(k * k, b1_ref[...], prec=HI) + EPS) * kw_ref[...]
        kr = kn * cos_ + _rot(kn) * sin_
        k4_ref[...] = _bdot(kr, dup_ref[...]).astype(BF16)
        v4_ref[...] = _bdot(v, dup_ref[...]).astype(BF16)

    row = lambda w, cb: pl.BlockSpec((TM, w), lambda i: (i, cb))
    out = jax.ShapeDtypeStruct((tt, ATW), BF16)
    return _pcall(body, name="qk_prep", grid=(tt // TM,),
                  in_specs=[row(ATW, C_QRAW), row(256, C_KV), row(128, 0), row(128, 0),
                            _full((1, ATW)), _full((1, 128)), _full((ATW, ATW)), _full((128, 128)),
                            _full((128, ATW))],
                  out_specs=[row(ATW, 0)] * 3, out_shape=[out] * 3)(
                      p, p, cos, sin, qnw8, knw2, bd512, bd128, dup)


def _attn_masks(i, nb):
    r = lax.broadcasted_iota(jnp.int32, (BLK, 3 * BLK + L), 0)
    c = lax.broadcasted_iota(jnp.int32, (BLK, 3 * BLK + L), 1)
    kpos = (i - 1) * BLK + c
    loc = (jnp.abs(c - BLK - r) <= BLK) & (kpos >= 0) & (kpos < nb * BLK)
    return loc | (c >= 3 * BLK)


def _lane_mask(j):
    lane = lax.broadcasted_iota(jnp.int32, (1, 256), 1)
    return (lane // HDIM) == j


def _attn_specs(nb):
    blk = lambda off: pl.BlockSpec((BLK, ATW), lambda i: (jnp.clip(i + off, 0, nb - 1) + 2, 0))
    ctx = pl.BlockSpec((L, ATW), lambda i: (0, 0))
    return blk, ctx


def _attn_fwd(qr, k4, v4, sinks, carry=None):
    tt = qr.shape[0]
    s_len = tt - L
    nb = s_len // BLK

    def body(sk_ref, q_ref, kp, ko, kn, kc, vp, vo, vn, vc, y_ref, lse_ref):
        i = pl.program_id(0)
        valid = _attn_masks(i, nb)
        q = q_ref[...]
        ys, lses = [], []
        for g in range(2):
            gs = slice(256 * g, 256 * g + 256)
            kcat = jnp.concatenate([kp[:, gs], ko[:, gs], kn[:, gs], kc[:, gs]], axis=0)
            vcat = jnp.concatenate([vp[:, gs], vo[:, gs], vn[:, gs], vc[:, gs]], axis=0)
            qg = q[:, gs]
            og = jnp.zeros((BLK, 256), F32)
            lg = jnp.zeros((BLK, 256), F32)
            for j in range(4):
                lm = _lane_mask(j)
                sink = sk_ref[4 * g + j]
                s = jnp.where(valid, _dot(jnp.where(lm, qg, jnp.zeros_like(qg)), kcat, NT), -1e30)
                m = jnp.maximum(jnp.max(s, axis=-1, keepdims=True), sink)
                e = jnp.exp(s - m)
                den = jnp.sum(e, axis=-1, keepdims=True) + jnp.exp(sink - m)
                pr = e / den
                og = og + jnp.where(lm, _bdot(pr, vcat), 0.0)
                lg = lg + jnp.where(lm, m + jnp.log(den), 0.0)
            ys.append(og)
            lses.append(lg)
        y_ref[...] = jnp.concatenate(ys, axis=1).astype(BF16)
        lse_ref[...] = jnp.concatenate(lses, axis=1)

    blk, ctx = _attn_specs(nb)
    out = pl.BlockSpec((BLK, ATW), lambda i: (i, 0))
    return _pcall(body, name="attn_fwd", grid=(nb,),
                  in_specs=[pl.BlockSpec(memory_space=pltpu.SMEM), blk(0),
                            blk(-1), blk(0), blk(1), ctx, blk(-1), blk(0), blk(1), ctx],
                  out_specs=[out, out],
                  out_shape=[jax.ShapeDtypeStruct((s_len, ATW), BF16),
                             jax.ShapeDtypeStruct((s_len, ATW), F32)], carry=carry)(
                      sinks, qr, k4, k4, k4, k4, v4, v4, v4, v4)


def _attn_bwd(qr, k4, v4, sinks, y, lse, dy, carry=None):
    tt = qr.shape[0]
    s_len = tt - L
    nb = s_len // BLK

    def body(sk_ref, q_ref, kp, ko, kn, kc, vp, vo, vn, vc, y_ref, lse_ref, dy_ref,
             dq_ref, dkw_ref, dvw_ref, dkc_ref, dvc_ref, dsk_ref):
        i = pl.program_id(0)

        @pl.when(i == 0)
        def _():
            dkc_ref[...] = jnp.zeros_like(dkc_ref)
            dvc_ref[...] = jnp.zeros_like(dvc_ref)
            dsk_ref[...] = jnp.zeros_like(dsk_ref)

        valid = _attn_masks(i, nb)
        q = q_ref[...]
        dy_ = dy_ref[...]
        dly = dy_ * y_ref[...].astype(F32)
        lse_ = lse_ref[...]
        dqs = []
        for g in range(2):
            gs = slice(256 * g, 256 * g + 256)
            kcat = jnp.concatenate([kp[:, gs], ko[:, gs], kn[:, gs], kc[:, gs]], axis=0)
            vcat = jnp.concatenate([vp[:, gs], vo[:, gs], vn[:, gs], vc[:, gs]], axis=0)
            qg, dyg, dlg, lsg = q[:, gs], dy_[:, gs], dly[:, gs], lse_[:, gs]
            dqg = jnp.zeros((BLK, 256), F32)
            dkg = jnp.zeros((3 * BLK + L, 256), F32)
            dvg = jnp.zeros((3 * BLK + L, 256), F32)
            for j in range(4):
                lm = _lane_mask(j)
                sink = sk_ref[4 * g + j]
                qm = jnp.where(lm, qg, jnp.zeros_like(qg))
                dym = jnp.where(lm, dyg, 0.0).astype(BF16)
                lse_h = jnp.max(jnp.where(lm, lsg, -1e30), axis=-1, keepdims=True)
                delta = jnp.sum(jnp.where(lm, dlg, 0.0), axis=-1, keepdims=True)
                s = _dot(qm, kcat, NT)
                pr = jnp.where(valid, jnp.exp(s - lse_h), 0.0)
                dpr = _dot(dym, vcat, NT)
                dsc = pr * (dpr - delta)
                psink = jnp.exp(sink - lse_h)
                dsk_ref[4 * g + j:4 * g + j + 1, :] += jnp.broadcast_to(
                    -jnp.sum(psink * delta, axis=0, keepdims=True), (1, 128))
                dsb = dsc.astype(BF16)
                dqg = dqg + jnp.where(lm, _dot(dsb, kcat), 0.0)
                dkg = dkg + _dot(dsb, qm, TN)
                dvg = dvg + _dot(pr.astype(BF16), dym, TN)
            dqs.append(dqg)
            dkw_ref[0, :, gs] = dkg[:3 * BLK]
            dvw_ref[0, :, gs] = dvg[:3 * BLK]
            dkc_ref[:, gs] += dkg[3 * BLK:]
            dvc_ref[:, gs] += dvg[3 * BLK:]
        dq_ref[...] = jnp.concatenate(dqs, axis=1)

    blk, ctx = _attn_specs(nb)
    out = pl.BlockSpec((BLK, ATW), lambda i: (i, 0))
    win = pl.BlockSpec((1, 3 * BLK, ATW), lambda i: (i, 0, 0))
    acc = _full((L, ATW))
    return _pcall(body, name="attn_bwd", grid=(nb,),
                  in_specs=[pl.BlockSpec(memory_space=pltpu.SMEM), blk(0),
                            blk(-1), blk(0), blk(1), ctx, blk(-1), blk(0), blk(1), ctx, out, out, out],
                  out_specs=[out, win, win, acc, acc, _full((8, 128))],
                  out_shape=[jax.ShapeDtypeStruct((s_len, ATW), F32),
                             jax.ShapeDtypeStruct((nb, 3 * BLK, ATW), F32),
                             jax.ShapeDtypeStruct((nb, 3 * BLK, ATW), F32),
                             jax.ShapeDtypeStruct((L, ATW), F32), jax.ShapeDtypeStruct((L, ATW), F32),
                             jax.ShapeDtypeStruct((8, 128), F32)], carry=carry)(
                      sinks, qr, k4, k4, k4, k4, v4, v4, v4, v4, y, lse, dy)


def _attn_post(p, cos, sin, qnw8, knw2, bd512, bd128, dupt, dq, dkw, dvw, dkc, dvc, dp, carry=None):
    tt = p.shape[0]
    s_len = tt - L
    nb = s_len // BLK
    nctx = L // BLK

    def body(q_ref, kv_ref, cos_ref, sin_ref, qw_ref, kw_ref, b5_ref, b1_ref, dupt_ref,
             dq_ref, kwp, kwo, kwn, vwp, vwo, vwn, dkc_ref, dvc_ref, _dp_in,
             dp_ref, dqw_ref, dkw_ref):
        t = pl.program_id(0)
        j = t - nctx

        @pl.when(t == 0)
        def _():
            dqw_ref[...] = jnp.zeros_like(dqw_ref)
            dkw_ref[...] = jnp.zeros_like(dkw_ref)

        is_lat = t >= nctx
        cos_, sin_ = cos_ref[...], sin_ref[...]
        has_p = is_lat & (j >= 1)
        has_n = is_lat & (j <= nb - 2)
        dk4 = (jnp.where(is_lat, kwo[0], dkc_ref[...]) + jnp.where(has_p, kwp[0], 0.0)
               + jnp.where(has_n, kwn[0], 0.0))
        dv4 = (jnp.where(is_lat, vwo[0], dvc_ref[...]) + jnp.where(has_p, vwp[0], 0.0)
               + jnp.where(has_n, vwn[0], 0.0))
        dkr = _dot(dk4, dupt_ref[...], prec=HI)
        dv = _dot(dv4, dupt_ref[...], prec=HI)
        kv = kv_ref[...]
        k = kv[:, :128]
        kw = kw_ref[...]
        rk = lax.rsqrt(_dot(k * k, b1_ref[...], prec=HI) + EPS)
        xk = k * rk
        dkn = dkr * cos_ + _rot(dkr * sin_)
        dxk = dkn * kw
        dk = rk * (dxk - xk * _dot(dxk * xk, b1_ref[...], prec=HI))
        dkw_ref[...] += jnp.sum(dkn * xk, axis=0, keepdims=True)
        q = q_ref[...]
        qw = qw_ref[...]
        rq = lax.rsqrt(_dot(q * q, b5_ref[...], prec=HI) + EPS)
        xq = q * rq
        cos4 = jnp.concatenate([cos_] * 4, axis=1)
        sin4 = jnp.concatenate([sin_] * 4, axis=1)
        dqr = jnp.where(is_lat, dq_ref[...], 0.0) * (HDIM ** -0.5)
        dqn = dqr * cos4 + _rot(dqr * sin4)
        dxq = dqn * qw
        dqraw = rq * (dxq - xq * _dot(dxq * xq, b5_ref[...], prec=HI))
        dqw_ref[...] += jnp.sum(dqn * xq, axis=0, keepdims=True)
        dp_ref[...] = jnp.concatenate([dqraw, dk, dv], axis=1).astype(BF16)

    row = lambda w, cb: pl.BlockSpec((BLK, w), lambda t: (t, cb))
    lat = pl.BlockSpec((BLK, ATW), lambda t: (jnp.maximum(t - nctx, 0), 0))

    def part(off):
        return pl.BlockSpec((1, BLK, ATW), lambda t: (jnp.clip(t - nctx + off, 0, nb - 1), 1 - off, 0))

    cacc = pl.BlockSpec((BLK, ATW), lambda t: (jnp.minimum(t, nctx - 1), 0))
    return _pcall(body, name="attn_post", grid=(tt // BLK,),
                  in_specs=[row(ATW, C_QRAW), row(256, C_KV), row(128, 0), row(128, 0),
                            _full((1, ATW)), _full((1, 128)), _full((ATW, ATW)), _full((128, 128)),
                            _full((ATW, 128)), lat, part(-1), part(0), part(1), part(-1), part(0), part(1),
                            cacc, cacc, ANY],
                  out_specs=[pl.BlockSpec((BLK, 768), lambda t: (t, C_QKV)), _full((1, ATW)), _full((1, 128))],
                  out_shape=[jax.ShapeDtypeStruct(dp.shape, BF16), jax.ShapeDtypeStruct((1, ATW), F32),
                             jax.ShapeDtypeStruct((1, 128), F32)],
                  aliases={18: 0}, carry=carry)(p, p, cos, sin, qnw8, knw2, bd512, bd128, dupt,
                                   dq, dkw, dkw, dkw, dvw, dvw, dvw, dkc, dvc, dp)


def _merge(ah, aa, p):
    s_len = ah.shape[0]

    def body(ah_ref, aa_ref, gh_ref, ga_ref, m_ref):
        m_ref[...] = (_sig(gh_ref[...]) * ah_ref[...] + _sig(ga_ref[...]) * aa_ref[...]).astype(BF16)

    row = pl.BlockSpec((TM, D), lambda i: (i, 0))
    return _pcall(body, name="merge", grid=(s_len // TM,),
                  in_specs=[row, row, pl.BlockSpec((TM, D), lambda i: (i + 1, 2)),
                            pl.BlockSpec((TM, D), lambda i: (i + 1, 3))],
                  out_specs=row, out_shape=jax.ShapeDtypeStruct((s_len, D), BF16))(ah, aa, p, p)


def _merge_bwd(dm, ah, aa, p, carry=None):
    tt = p.shape[0]
    s_len = tt - L

    def body(dm_ref, ah_ref, aa_ref, gh_ref, ga_ref, dp_ref, dmh_ref, dma_ref):
        i = pl.program_id(0)

        @pl.when(i == 0)
        def _():
            dp_ref[...] = jnp.zeros_like(dp_ref)

        @pl.when(i >= 1)
        def _():
            dm_ = dm_ref[...]
            sh, sa = _sig(gh_ref[...]), _sig(ga_ref[...])
            dp_ref[...] = jnp.concatenate([dm_ * ah_ref[...] * sh * (1.0 - sh),
                                           dm_ * aa_ref[...] * sa * (1.0 - sa)], axis=1).astype(BF16)
            dmh_ref[...] = (dm_ * sh).astype(BF16)
            dma_ref[...] = (dm_ * sa).astype(BF16)

    lat = pl.BlockSpec((TM, D), lambda i: (jnp.maximum(i - 1, 0), 0))
    return _pcall(body, name="merge_bwd", grid=(tt // TM,),
                  in_specs=[lat, lat, lat, pl.BlockSpec((TM, D), lambda i: (i, 2)),
                            pl.BlockSpec((TM, D), lambda i: (i, 3))],
                  out_specs=[pl.BlockSpec((TM, 2 * D), lambda i: (i, C_GATES)), lat, lat],
                  out_shape=[jax.ShapeDtypeStruct((tt, NCOL), BF16), jax.ShapeDtypeStruct((s_len, D), BF16),
                             jax.ShapeDtypeStruct((s_len, D), BF16)], carry=carry)(dm, ah, aa, p, p)


def _local_step(x, ctx, tgt, mod, modc, nw1, nw2, lg, hw, qnw, knw, sinks,
                w_in, wts, dist=None):
    s_len = x.shape[0]
    tt = s_len + L
    tok = jnp.concatenate([ctx, x], axis=0)
    ss1 = jnp.stack([modc, mod[0:2]])
    ss2 = mod[3:5][None]
    g1, g2 = mod[2:3], mod[5:6]
    hw4 = jnp.tile(hw, (1, 4))
    qnw8 = jnp.tile(qnw, (1, 8))
    knw2 = jnp.tile(knw, (1, 2))
    cos, sin = _rope_tables(s_len)
    bd512, bd128 = _blockdiag(ATW, HDIM), _blockdiag(128, HDIM)
    dupm = _dup_matrix()
    dup, dupt = jnp.asarray(dupm, BF16), jnp.asarray(dupm.T, F32)
    tmt = tt

    def four(b):
        return b.reshape(4, 2 * b.shape[1], b.shape[2])

    def halves(g):
        return g.reshape(4, 2, g.shape[1] // 2, g.shape[2])

    h = _modulate(tok, nw1, ss1, name="mod1", sel=lambda i: jnp.minimum(i, 1))
    if dist is None:
        bh4, ba4, w_o, g4, u4, dn4 = wts
        p = _mm_in(h, w_in, tmt)
        o0, st0 = _hgrn_fwd(p, lg, rev=False)
        o1, st1 = _hgrn_fwd(p, lg, rev=True)
    else:
        core, chip = dist
        p, first = _mm_in(h, w_in, tmt, carry=_carry_gather(list(wts[0:3])))
        (o0, st0), (g8,) = _hgrn_fwd(p, lg, rev=False, carry=_carry_gather([wts[3]]))
        o1, st1 = _hgrn_fwd(p, lg, rev=True)
        bh4, ba4, w_o, g4 = four(first[0]), four(first[1]), four(first[2]).reshape(D, D), four(g8)
    y_hg = _readout(o0, o1, p, hw4)
    qr, k4, v4 = _qk_prep(p, cos, sin, qnw8, knw2, bd512, bd128, dup)
    if dist is None:
        y_at, lse = _attn_fwd(qr, k4, v4, sinks)
    else:
        (y_at, lse), (u8, dn8) = _attn_fwd(qr, k4, v4, sinks, carry=_carry_gather(list(wts[4:6])))
        u4, dn4 = four(u8), four(dn8)
    ah = _mm_cs(y_hg, bh4, name="mm_bh")
    aa = _mm_cs(y_at, ba4, name="mm_ba")
    mixed = _merge(ah, aa, p)
    ao = _mm(mixed, w_o, name="mm_o", tm=512, tn=D, tk=D)
    x1, h2 = _res1_mod2(x, ao, g1, nw2, ss2)
    a4, b4, z4 = _ffn_up(h2, g4, u4)
    y = _ffn_down(z4, dn4)
    sq, dx2, dyb, dg2 = _loss_head(x1, y, g2, tgt)

    da4, db4 = _ffn_dz(dyb, dn4, a4, b4)
    g_dn = _ffn_gdn(z4, dyb)
    dh2 = _ffn_dh2(da4, db4, g4, u4)
    g_g, g_u = _ffn_ggu(h2, da4, db4)
    dx1, dattn, dss2, dnw2, dg1 = _mod2_bwd(x1, dh2, dx2, ao, nw2, ss2, g1)
    dm = _mm(dattn, w_o, name="mm_dm", mode="nt", tm=512, tn=D, tk=D)
    g_o = _mm(mixed, dattn, name="mm_go", mode="tn", tm=D, tn=D, tk=512)
    if dist is None:
        dp, dmh, dma = _merge_bwd(dm, ah, aa, p)
    else:
        ffn_units = [halves(g_dn), halves(g_g), halves(g_u)]
        (dp, dmh, dma), ffn_recv = _merge_bwd(dm, ah, aa, p, carry=_carry_pairx(ffn_units))
        ffn_pairs = _rs_pair_add(ffn_units, ffn_recv, core)
    dy_hg = _mm_cs_nt(dmh, bh4, name="mm_dyh")
    dy_at = _mm_cs_nt(dma, ba4, name="mm_dya")
    g_bh = _mm_cs_tn(y_hg, dmh, D // 4, name="mm_gbh")
    g_ba = _mm_cs_tn(y_at, dma, D // 4, name="mm_gba")
    if dist is None:
        dp, do, dhw4 = _readout_bwd(o0, o1, p, hw4, dy_hg, dp)
        dq, dkw, dvw, dkc, dvc, dsk = _attn_bwd(qr, k4, v4, sinks, y_at, lse, dy_at)
        dp, dqnw8, dknw2 = _attn_post(p, cos, sin, qnw8, knw2, bd512, bd128, dupt, dq, dkw, dvw, dkc, dvc, dp)
    else:
        mix_units = [halves(g_bh), halves(g_ba), halves(g_o.reshape(4, D // 4, D))]
        (dp, do, dhw4), mix_recv = _readout_bwd(o0, o1, p, hw4, dy_hg, dp, carry=_carry_pairx(mix_units))
        mix_pairs = _rs_pair_add(mix_units, mix_recv, core)
        (dq, dkw, dvw, dkc, dvc, dsk), contribs_a = _attn_bwd(qr, k4, v4, sinks, y_at, lse, dy_at,
                                                              carry=_carry_chipx(ffn_pairs[0:2]))
        reds_a = _rs_chip_add(ffn_pairs[0:2], contribs_a, core, chip)
        (dp, dqnw8, dknw2), post = _attn_post(
            p, cos, sin, qnw8, knw2, bd512, bd128, dupt, dq, dkw, dvw, dkc, dvc, dp,
            carry=_carry_join(_carry_chipx(ffn_pairs[2:3]), _carry_sibx(reds_a)))
        reds_b = _rs_chip_add(ffn_pairs[2:3], post[0:1], core, chip)
    if dist is None:
        dp, dv0, dq0, dlg0 = _hgrn_bwd(p, lg, do, st0, dp, None, rev=False)
        dp, dlg1 = _hgrn_bwd(p, lg, do, st1, dp, (dv0, dq0), rev=True)
    else:
        (dp, dv0, dq0, dlg0), mid = _hgrn_bwd(p, lg, do, st0, dp, None, rev=False,
                                              carry=_carry_join(_carry_chipx(mix_pairs), _carry_sibx(reds_b)))
        mix_reds = _rs_chip_add(mix_pairs, mid[0:3], core, chip)
        (dp, dlg1), mix_done = _hgrn_bwd(p, lg, do, st1, dp, (dv0, dq0), rev=True, carry=_carry_sibx(mix_reds))
        ffn_done = post[1:3] + mid[3:4]
    dh = _mm_dh(dp, w_in, tmt)
    g_in = _mm_gin(dp, h, tmt)
    if dist is None:
        gx, dss1, dnw1 = _mod1_bwd(tok, dh, dx1, nw1, ss1)
        rs = None
    else:
        in_units = [halves(g_in.reshape(4, NCOL // 4, D))]
        (gx, dss1, dnw1), in_recv = _mod1_bwd(tok, dh, dx1, nw1, ss1, carry=_carry_pairx(in_units))
        rs = dict(ffn_done=ffn_done, mix_done=mix_done, in_units=in_units, in_recv=in_recv)

    dmod = jnp.concatenate([dss1[1], dg1, dss2, dg2], axis=0)
    dmodc = dss1[0]
    raw = (dss1, dg1, dss2, dg2, dnw1, dnw2, dhw4, dqnw8, dknw2, dsk, dlg0, dlg1)
    small = dict(raw=raw, dmod=dmod, dmodc=dmodc, dnw1=dnw1, dnw2=dnw2,
                 dhw=dhw4.reshape(4, HGD).sum(0, keepdims=True),
                 dqnw=dqnw8.reshape(8, HDIM).sum(0, keepdims=True),
                 dknw=dknw2.reshape(2, HDIM).sum(0, keepdims=True),
                 dsinks=dsk[:, 0], dlg=jnp.concatenate([dlg0, dlg1], axis=0))
    big = dict(w_in=g_in, w_bh=g_bh, w_ba=g_ba, w_o=g_o, w_g=g_g, w_u=g_u, w_dn=g_dn)
    return sq, gx, big, small, rs


def _place():
    x, y, c = lax.axis_index("x"), lax.axis_index("y"), lax.axis_index("c")
    return x, y, c


def _gather_blocks(x_refs, out_refs, send_sems, recv_sems, local_sems):
    n = len(out_refs)
    x, y, c = _place()
    me, sibling = (x, y, c), (x, y, 1 - c)
    chips = [(1 - x, y), (x, 1 - y), (1 - x, 1 - y)]

    def slot(u, px, py, pc):
        return out_refs[u].at[4 * px + 2 * py + pc]

    def copy(u, k, block, to, src=None):
        return pltpu.make_async_remote_copy(
            src_ref=slot(u, *block) if src is None else src, dst_ref=slot(u, *block),
            send_sem=send_sems.at[u, k], recv_sem=recv_sems.at[u, k], device_id=to, device_id_type=MESH)

    mines = []
    if x_refs is not None:
        mines = [pltpu.make_async_copy(x_refs[u], slot(u, *me), local_sems.at[u]) for u in range(n)]
    for cp in mines:
        cp.start()
    first = []
    for u in range(n):
        src = None if x_refs is None else x_refs[u]
        first.append(copy(u, 0, me, sibling, src=src))
        first += [copy(u, 1 + j, me, (*chip, c), src=src) for j, chip in enumerate(chips)]
    for cp in first:
        cp.start()
    passed = []
    for j, chip in enumerate(chips):
        for u in range(n):
            copy(u, 1 + j, (*chip, c), me).wait_recv()
            fwd = copy(u, 4 + j, (*chip, c), sibling)
            fwd.start()
            passed.append(fwd)
    for u in range(n):
        copy(u, 0, sibling, me).wait_recv()
    for j, chip in enumerate(chips):
        for u in range(n):
            copy(u, 4 + j, (*chip, 1 - c), me).wait_recv()
    for cp in first + passed:
        cp.wait_send()
    for cp in mines:
        cp.wait()


def _gather_sems(n):
    return [pltpu.SemaphoreType.DMA((n, 7)), pltpu.SemaphoreType.DMA((n, 7)), pltpu.SemaphoreType.DMA((n,))]


def _allgather(blks, *, name, in_vmem):
    n = len(blks)
    space = pltpu.VMEM if in_vmem else pl.ANY

    def body(*refs):
        _gather_blocks(refs[:n], refs[n:2 * n], *refs[2 * n:])

    return pl.pallas_call(
        body, name=name, out_shape=[jax.ShapeDtypeStruct((8,) + b.shape, b.dtype) for b in blks],
        in_specs=[pl.BlockSpec(memory_space=space)] * n, out_specs=[pl.BlockSpec(memory_space=space)] * n,
        scratch_shapes=_gather_sems(n))(*blks)


def _cast_place(ws, c, dev):
    n = len(ws)

    def body(s_ref, *refs):
        for u in range(n):
            refs[n + u][0] = refs[u][...].astype(BF16)

    in_specs, out_specs, out_shape = [], [], []
    for w in ws:
        q, cols = w.shape[0] // 4, w.shape[1]
        in_specs.append(pl.BlockSpec((q, cols), lambda i, s: (2 * s[0] + i, 0)))
        out_specs.append(pl.BlockSpec((1, q, cols), lambda i, s: (s[1], i, 0)))
        out_shape.append(jax.ShapeDtypeStruct((8, 2 * q, cols), BF16))
    return pl.pallas_call(
        body, name="cast_place",
        grid_spec=pltpu.PrefetchScalarGridSpec(num_scalar_prefetch=1, grid=(2,), in_specs=in_specs,
                                               out_specs=out_specs),
        out_shape=out_shape,
        compiler_params=pltpu.CompilerParams(vmem_limit_bytes=48 << 20))(jnp.stack([c, dev]), *ws)


def _gather_phases(out_refs, send_sems, recv_sems):
    n = len(out_refs)
    x, y, c = _place()
    me, sibling = (x, y, c), (x, y, 1 - c)
    chips = [(1 - x, y), (x, 1 - y), (1 - x, 1 - y)]

    def copy(u, k, block, to):
        px, py, pc = block
        ref = out_refs[u].at[4 * px + 2 * py + pc]
        return pltpu.make_async_remote_copy(src_ref=ref, dst_ref=ref, send_sem=send_sems.at[u, k],
                                            recv_sem=recv_sems.at[u, k], device_id=to, device_id_type=MESH)

    def start():
        for u in range(n):
            copy(u, 0, me, sibling).start()
            for j, chip in enumerate(chips):
                copy(u, 1 + j, me, (*chip, c)).start()

    def mid():
        for j, chip in enumerate(chips):
            for u in range(n):
                copy(u, 1 + j, (*chip, c), me).wait_recv()
                copy(u, 4 + j, (*chip, c), sibling).start()

    def end():
        for u in range(n):
            copy(u, 0, sibling, me).wait_recv()
        for j, chip in enumerate(chips):
            for u in range(n):
                copy(u, 4 + j, (*chip, 1 - c), me).wait_recv()
        for u in range(n):
            copy(u, 0, me, sibling).wait_send()
            for j, chip in enumerate(chips):
                copy(u, 1 + j, me, (*chip, c)).wait_send()
                copy(u, 4 + j, (*chip, c), sibling).wait_send()

    return start, mid, end


def _carry_gather(bufs):
    n = len(bufs)
    return _Carry(bufs, [jax.ShapeDtypeStruct(b.shape, b.dtype) for b in bufs], {u: u for u in range(n)},
                  [pltpu.SemaphoreType.DMA((n, 7)), pltpu.SemaphoreType.DMA((n, 7))],
                  lambda ins, outs, sems: _gather_phases(outs, *sems))


def _allgather_inplace(bufs, *, name):
    n = len(bufs)

    def body(*refs):
        for phase in _gather_phases(refs[n:2 * n], *refs[2 * n:]):
            phase()

    return pl.pallas_call(
        body, name=name, out_shape=[jax.ShapeDtypeStruct(b.shape, b.dtype) for b in bufs],
        in_specs=[ANY] * n, out_specs=[ANY] * n, input_output_aliases={u: u for u in range(n)},
        scratch_shapes=[pltpu.SemaphoreType.DMA((n, 7)), pltpu.SemaphoreType.DMA((n, 7))])(*bufs)


def _ag_small(raw):
    def body(dss1, dg1, dss2, dg2, dnw1, dnw2, dhw4, dqnw8, dknw2, dsk, dlg0, dlg1,
             out_ref, tot_ref, blk, send_sems, recv_sems, local_sems):
        blk[...] = jnp.zeros_like(blk)
        blk[0:2, :] = dss1[1]
        blk[2:3, :] = dg1[...]
        blk[3:5, :] = dss2[...]
        blk[5:6, :] = dg2[...]
        blk[6:8, :] = dss1[0]
        blk[8:9, :] = dnw1[...]
        blk[9:10, :] = dnw2[...]
        blk[10:11, 0:HGW] = dhw4[...]
        blk[10:11, HGW:D] = dqnw8[...]
        blk[11:12, 0:128] = dknw2[...]
        blk[12:14, 0:HGW] = dlg0[0]
        blk[14:16, 0:HGW] = dlg1[0]
        blk[16:24, 0:128] = dsk[...]
        _gather_blocks([blk], [out_ref], send_sems, recv_sems, local_sems)
        acc = out_ref[0]
        for i in range(1, 8):
            acc = acc + out_ref[i]
        tot_ref[...] = acc

    vm = pl.BlockSpec(memory_space=pltpu.VMEM)
    return pl.pallas_call(
        body, name="ag_small",
        out_shape=[jax.ShapeDtypeStruct((8, 24, D), F32), jax.ShapeDtypeStruct((24, D), F32)],
        in_specs=[vm] * 12, out_specs=[vm, vm],
        scratch_shapes=[pltpu.VMEM((24, D), F32)] + _gather_sems(1))(*raw)


def _rs_pair_exchange(units):
    n = len(units)

    def body(*refs):
        start, _, end = _pairx_phases(refs[:n], refs[n:2 * n], *refs[2 * n:])
        start()
        end()

    return pl.pallas_call(
        body, name="rs_pair_exchange", out_shape=_pairx_shapes(units),
        in_specs=[ANY] * n, out_specs=[ANY] * n,
        scratch_shapes=[pltpu.SemaphoreType.DMA((n, 4)), pltpu.SemaphoreType.DMA((n, 4))])(*units)


def _pairx_shapes(units):
    return [jax.ShapeDtypeStruct((4,) + g.shape[2:], g.dtype) for g in units]


def _pairx_phases(g_refs, r_refs, send_sems, recv_sems):
    n = len(g_refs)
    x, y, c = _place()
    cps = [pltpu.make_async_remote_copy(
        src_ref=g_refs[u].at[j, 1 - c], dst_ref=r_refs[u].at[j], send_sem=send_sems.at[u, j],
        recv_sem=recv_sems.at[u, j], device_id=(x, y, 1 - c), device_id_type=MESH)
        for u in range(n) for j in range(4)]

    def start():
        for cp in cps:
            cp.start()

    def end():
        for cp in cps:
            cp.wait()

    return start, None, end


def _carry_pairx(units):
    n = len(units)
    return _Carry(units, _pairx_shapes(units), {},
                  [pltpu.SemaphoreType.DMA((n, 4)), pltpu.SemaphoreType.DMA((n, 4))],
                  lambda ins, outs, sems: _pairx_phases(ins, outs, *sems))


def _rs_pair_add(units, recvs, c):
    n = len(units)

    def body(c_ref, *refs):
        for u in range(n):
            refs[2 * n + u][...] = (refs[u][0] + refs[n + u][...]).astype(BF16)

    in_specs, out_specs, out_shape = [], [], []
    for g in units:
        h, w = g.shape[2] // 2, g.shape[3]
        in_specs.append(pl.BlockSpec((1, 1, h, w), lambda j, i, cr: (j, cr[0], i, 0)))
    for g in units:
        h, w = g.shape[2] // 2, g.shape[3]
        in_specs.append(pl.BlockSpec((1, h, w), lambda j, i, cr: (j, i, 0)))
        out_specs.append(pl.BlockSpec((1, h, w), lambda j, i, cr: (j, i, 0)))
        out_shape.append(jax.ShapeDtypeStruct((4, 2 * h, w), BF16))
    return pl.pallas_call(
        body, name="rs_pair_add",
        grid_spec=pltpu.PrefetchScalarGridSpec(num_scalar_prefetch=1, grid=(4, 2), in_specs=in_specs,
                                               out_specs=out_specs),
        out_shape=out_shape,
        compiler_params=pltpu.CompilerParams(vmem_limit_bytes=48 << 20))(c.reshape(1), *units, *recvs)


def _rs_chip_exchange(pairs):
    n = len(pairs)

    def body(*refs):
        start, _, end = _chipx_phases(refs[:n], refs[n:2 * n], *refs[2 * n:])
        start()
        end()

    return pl.pallas_call(
        body, name="rs_chip_exchange", out_shape=[jax.ShapeDtypeStruct(p.shape, p.dtype) for p in pairs],
        in_specs=[ANY] * n, out_specs=[ANY] * n,
        scratch_shapes=[pltpu.SemaphoreType.DMA((n, 3)), pltpu.SemaphoreType.DMA((n, 3))])(*pairs)


def _chipx_phases(p_refs, r_refs, send_sems, recv_sems):
    n = len(p_refs)
    x, y, c = _place()
    k = 2 * x + y
    sends = []
    for d in range(1, 4):
        j = (k + d) % 4
        for u in range(n):
            sends.append(pltpu.make_async_remote_copy(
                src_ref=p_refs[u].at[j], dst_ref=r_refs[u].at[k], send_sem=send_sems.at[u, d - 1],
                recv_sem=recv_sems.at[u, d - 1], device_id=(j // 2, j % 2, c), device_id_type=MESH))

    def start():
        for cp in sends:
            cp.start()

    def end():
        for d in range(1, 4):
            src = (k + 4 - d) % 4
            for u in range(n):
                pltpu.make_async_remote_copy(
                    src_ref=p_refs[u].at[src], dst_ref=r_refs[u].at[src], send_sem=send_sems.at[u, d - 1],
                    recv_sem=recv_sems.at[u, d - 1], device_id=(x, y, c), device_id_type=MESH).wait_recv()
        for cp in sends:
            cp.wait_send()

    return start, None, end


def _carry_chipx(pairs):
    n = len(pairs)
    return _Carry(pairs, [jax.ShapeDtypeStruct(p.shape, p.dtype) for p in pairs], {},
                  [pltpu.SemaphoreType.DMA((n, 3)), pltpu.SemaphoreType.DMA((n, 3))],
                  lambda ins, outs, sems: _chipx_phases(ins, outs, *sems))


def _rs_chip_add(pairs, contribs, c, chip):
    n = len(pairs)

    def body(s_ref, *refs):
        for u in range(n):
            a, b, c_, d = refs[4 * u:4 * u + 4]
            refs[4 * n + u][0] = ((a[0].astype(F32) + b[0].astype(F32)) + c_[0].astype(F32)) + d[0].astype(F32)

    in_specs, out_specs, out_shape, args = [], [], [], []
    for p, r in zip(pairs, contribs):
        h, w = p.shape[1] // 2, p.shape[2]
        in_specs += [pl.BlockSpec((1, h, w), functools.partial(lambda d, i, s: ((s[1] + d) % 4, i, 0), d))
                     for d in range(4)]
        args += [p, r, r, r]
        out_specs.append(pl.BlockSpec((1, h, w), lambda i, s: (s[0], i, 0)))
        out_shape.append(jax.ShapeDtypeStruct((2, 2 * h, w), F32))
    return pl.pallas_call(
        body, name="rs_chip_add",
        grid_spec=pltpu.PrefetchScalarGridSpec(num_scalar_prefetch=1, grid=(2,), in_specs=in_specs,
                                               out_specs=out_specs),
        out_shape=out_shape,
        compiler_params=pltpu.CompilerParams(vmem_limit_bytes=48 << 20))(jnp.stack([c, chip]), *args)


def _rs_sibling_gather(reds):
    n = len(reds)

    def body(*refs):
        start, _, end = _sibx_phases(refs[n:2 * n], *refs[2 * n:])
        start()
        end()

    return pl.pallas_call(
        body, name="rs_sibling_gather", out_shape=[jax.ShapeDtypeStruct(r.shape, r.dtype) for r in reds],
        in_specs=[ANY] * n, out_specs=[ANY] * n, input_output_aliases={u: u for u in range(n)},
        scratch_shapes=[pltpu.SemaphoreType.DMA((n,))] * 2)(*reds)


def _sibx_phases(o_refs, send_sems, recv_sems):
    n = len(o_refs)
    x, y, c = _place()
    cps = [pltpu.make_async_remote_copy(
        src_ref=o_refs[u].at[c], dst_ref=o_refs[u].at[c], send_sem=send_sems.at[u], recv_sem=recv_sems.at[u],
        device_id=(x, y, 1 - c), device_id_type=MESH) for u in range(n)]

    def start():
        for cp in cps:
            cp.start()

    def end():
        for u in range(n):
            cps[u].wait_send()
            pltpu.make_async_remote_copy(
                src_ref=o_refs[u].at[1 - c], dst_ref=o_refs[u].at[1 - c], send_sem=send_sems.at[u],
                recv_sem=recv_sems.at[u], device_id=(x, y, 1 - c), device_id_type=MESH).wait_recv()

    return start, None, end


def _carry_sibx(reds):
    n = len(reds)
    return _Carry(reds, [jax.ShapeDtypeStruct(r.shape, r.dtype) for r in reds], {u: u for u in range(n)},
                  [pltpu.SemaphoreType.DMA((n,))] * 2, lambda ins, outs, sems: _sibx_phases(outs, *sems))


def _ada_fwd(c16, w, b):
    n = w.shape[1]
    tn = 512

    def body(c_ref, w_ref, b_ref, o_ref):
        cc = c_ref[...]
        o_ref[...] = _dot(cc * _sig(cc), w_ref[...], prec=HI) + b_ref[...]

    return _pcall(body, name="ada_fwd", grid=(n // tn,),
                  in_specs=[_full((16, D)), pl.BlockSpec((D, tn), lambda j: (0, j)),
                            pl.BlockSpec((1, tn), lambda j: (0, j))],
                  out_specs=pl.BlockSpec((16, tn), lambda j: (0, j)),
                  out_shape=jax.ShapeDtypeStruct((16, n), F32))(c16, w, b)


def _ada_bwd(c16, dmod16, w):
    n = w.shape[1]
    tn = 512

    def body(c_ref, d_ref, w_ref, gw_ref, gc_ref):
        j = pl.program_id(0)

        @pl.when(j == 0)
        def _():
            gc_ref[...] = jnp.zeros_like(gc_ref)

        cc = c_ref[...]
        dm = d_ref[...]
        gw_ref[...] = _dot(cc * _sig(cc), dm, TN, prec=HI)
        gc_ref[...] += _dot(dm, w_ref[...], NT, prec=HI)

    return _pcall(body, name="ada_bwd", grid=(n // tn,),
                  in_specs=[_full((16, D)), pl.BlockSpec((16, tn), lambda j: (0, j)),
                            pl.BlockSpec((D, tn), lambda j: (0, j))],
                  out_specs=[pl.BlockSpec((D, tn), lambda j: (0, j)), _full((16, D))],
                  out_shape=[jax.ShapeDtypeStruct((D, n), F32),
                             jax.ShapeDtypeStruct((16, D), F32)])(c16, dmod16, w)


def _adam_math(w, g, m, v):
    c1 = 1.0 - ADAM_B1 ** ADAM_STEP
    c2 = 1.0 - ADAM_B2 ** ADAM_STEP
    nm = ADAM_B1 * m + (1.0 - ADAM_B1) * g
    nv = ADAM_B2 * v + (1.0 - ADAM_B2) * (g * g)
    return -ADAM_LR * ((nm / c1) / (jnp.sqrt(nv / c2) + ADAM_EPS) + ADAM_WD * w), nm, nv


def _adamw_small(ws, gs, ms, vs):
    n = len(ws)

    def body(*refs):
        for u in range(n):
            d_, nm, nv = _adam_math(refs[u][...], refs[n + u][...], refs[2 * n + u][...], refs[3 * n + u][...])
            refs[4 * n + u][...] = d_
            refs[5 * n + u][...] = nm
            refs[6 * n + u][...] = nv

    specs = [_full(w.shape) for w in ws]
    shapes = [jax.ShapeDtypeStruct(w.shape, F32) for w in ws]
    out = _pcall(body, name="adamw_small", grid=(1,), in_specs=specs * 4, out_specs=specs * 3,
                 out_shape=shapes * 3)(*ws, *gs, *ms, *vs)
    return out[:n], out[n:2 * n], out[2 * n:]


def _cctx_grad(parts, c_ctx):
    def body(p_ref, c_ref, o_ref):
        acc = p_ref[0:1, :]
        for k in range(1, 4):
            acc = acc + p_ref[k:k + 1, :]
        cc = c_ref[...]
        s = _sig(cc)
        o_ref[...] = acc * (s * (1.0 + cc * (1.0 - s)))

    return _pcall(body, name="cctx_grad", grid=(1,), in_specs=[_full(parts.shape), _full((1, D))],
                  out_specs=_full((1, D)), out_shape=jax.ShapeDtypeStruct((1, D), F32))(parts, c_ctx)


ADAM_STEPS = 8


def _adamw_multi(ws, gs, ms, vs, *, name, carry=None):
    n = len(ws)

    def body(*refs):
        for u in range(n):
            refs[4 * n + u][...], refs[5 * n + u][...], refs[6 * n + u][...] = _adam_math(
                refs[u][...], refs[n + u][...], refs[2 * n + u][...], refs[3 * n + u][...])

    specs = [pl.BlockSpec((w.shape[0] // ADAM_STEPS, w.shape[1]), lambda i: (i, 0)) for w in ws]
    shapes = [jax.ShapeDtypeStruct(w.shape, F32) for w in ws]
    res = _pcall(body, name=name, grid=(ADAM_STEPS,), in_specs=specs * 4, out_specs=specs * 3,
                 out_shape=shapes * 3, carry=carry)(*ws, *gs, *ms, *vs)
    out, extra = res if carry is not None else (res, None)
    return (out[:n], out[n:2 * n], out[2 * n:]), extra


def kernel(x, c, ctx, c_ctx, w_ada, b_ada, norm_mix_w, norm_ffn_w, w_in, hgrn_lb_logits, hgrn_norm_w, q_norm_w, k_norm_w, attn_sinks, w_branch_hgrn, w_branch_attn, w_out, w_ffn_gate, w_ffn_up, w_ffn_down, loss_target, m_c_ctx, m_w_ada, m_b_ada, m_norm_mix_w, m_norm_ffn_w, m_w_in, m_hgrn_lb_logits, m_hgrn_norm_w, m_q_norm_w, m_k_norm_w, m_attn_sinks, m_w_branch_hgrn, m_w_branch_attn, m_w_out, m_w_ffn_gate, m_w_ffn_up, m_w_ffn_down, v_c_ctx, v_w_ada, v_b_ada, v_norm_mix_w, v_norm_ffn_w, v_w_in, v_hgrn_lb_logits, v_hgrn_norm_w, v_q_norm_w, v_k_norm_w, v_attn_sinks, v_w_branch_hgrn, v_w_branch_attn, v_w_out, v_w_ffn_gate, v_w_ffn_up, v_w_ffn_down):
    xi, yi, ci = _place()
    chip = 2 * xi + yi
    dev = 2 * chip + ci
    s_len = x.shape[1]

    lbrow = jnp.pad(hgrn_lb_logits.reshape(1, 512), ((0, 0), (0, D - 512)))
    blk = jnp.concatenate([c, lbrow, jnp.zeros((6, D), F32)], axis=0)
    g0, = _allgather([blk], name="ag_cond", in_vmem=True)
    c16 = jnp.concatenate([g0[:, 0], c_ctx[None], jnp.zeros((7, D), F32)], axis=0)
    lg = g0[0::2, 1, :512].reshape(4, 2, 2, 128).transpose(1, 2, 0, 3).reshape(2, 2, HGW)

    nada = w_ada.shape[2]
    b_sh = lax.dynamic_slice(b_ada, (0, chip * nada), (1, nada))
    mod_sh = _ada_fwd(c16, w_ada[0], b_sh)
    g1, = _allgather([mod_sh], name="ag_mod", in_vmem=True)
    modall = g1[0::2].transpose(1, 0, 2).reshape(16, 4 * nada)
    mod = lax.dynamic_slice(modall, (dev, 0), (1, 6 * D)).reshape(6, D)
    modc = modall[8].reshape(6, D)[:2]

    shards = [w_in[0].T, w_branch_hgrn[0], w_branch_attn[0], w_out[0], w_ffn_gate[0].T, w_ffn_up[0].T,
              w_ffn_down[0]]
    bufs = _cast_place(shards, ci, dev)
    in8, = _allgather_inplace(bufs[0:1], name="ag_w_in")

    sq, gx, _, small, rs = _local_step(
        x[0], ctx[0], loss_target[0], mod, modc, norm_mix_w, norm_ffn_w, lg, hgrn_norm_w, q_norm_w,
        k_norm_w, attn_sinks[0], in8.reshape(NCOL, D), bufs[1:], dist=(ci, chip))
    loss = lax.psum(0.5 * jnp.sum(sq) / D, ("x", "y", "c"))

    def whole(r):
        return r.reshape(2 * r.shape[1], r.shape[2])

    g_dn, g_g, g_u = [whole(r) for r in rs["ffn_done"]]
    g_bh, g_ba, g_o = [whole(r) for r in rs["mix_done"]]
    in_pairs = _rs_pair_add(rs["in_units"], rs["in_recv"], ci)

    g2, tot = _ag_small(small["raw"])
    dmodc_tot = jnp.pad(tot[6:8].reshape(1, 2 * D), ((0, 0), (0, 4 * D)))
    g_b_ada = tot[0:6].reshape(1, 6 * D) + dmodc_tot
    dmod16 = jnp.concatenate([g2[:, 0:6].reshape(8, 6 * D), dmodc_tot, jnp.zeros((7, 6 * D), F32)], axis=0)
    g_w_ada, gc_part = _ada_bwd(c16, lax.dynamic_slice(dmod16, (0, chip * nada), (16, nada)), w_ada[0])
    g3, = _allgather([gc_part[8:16]], name="ag_cctx", in_vmem=True)
    g_c_ctx = _cctx_grad(g3[0::2, 0], c_ctx[None])[0]
    g_nw1 = tot[8:9]
    g_nw2 = tot[9:10]
    g_hw = tot[10, :HGW].reshape(4, HGD).sum(0, keepdims=True)
    g_qnw = tot[10, HGW:].reshape(8, HDIM).sum(0, keepdims=True)
    g_knw = tot[11, :128].reshape(2, HDIM).sum(0, keepdims=True)
    g_sinks = tot[16:24, 0][None]
    g_lg = lax.dynamic_slice(tot[12:16, :HGW].reshape(2, 2, HGW), (0, 0, chip * 128), (2, 2, 128))

    names = ["c_ctx", "w_ada", "b_ada", "norm_mix_w", "norm_ffn_w", "w_in", "hgrn_lb_logits", "hgrn_norm_w",
             "q_norm_w", "k_norm_w", "attn_sinks", "w_branch_hgrn", "w_branch_attn", "w_out", "w_ffn_gate",
             "w_ffn_up", "w_ffn_down"]
    ws = dict(zip(names, [c_ctx, w_ada, b_ada, norm_mix_w, norm_ffn_w, w_in, hgrn_lb_logits, hgrn_norm_w,
                          q_norm_w, k_norm_w, attn_sinks, w_branch_hgrn, w_branch_attn, w_out, w_ffn_gate,
                          w_ffn_up, w_ffn_down]))
    ms = dict(zip(names, [m_c_ctx, m_w_ada, m_b_ada, m_norm_mix_w, m_norm_ffn_w, m_w_in, m_hgrn_lb_logits,
                          m_hgrn_norm_w, m_q_norm_w, m_k_norm_w, m_attn_sinks, m_w_branch_hgrn,
                          m_w_branch_attn, m_w_out, m_w_ffn_gate, m_w_ffn_up, m_w_ffn_down]))
    vs = dict(zip(names, [v_c_ctx, v_w_ada, v_b_ada, v_norm_mix_w, v_norm_ffn_w, v_w_in, v_hgrn_lb_logits,
                          v_hgrn_norm_w, v_q_norm_w, v_k_norm_w, v_attn_sinks, v_w_branch_hgrn,
                          v_w_branch_attn, v_w_out, v_w_ffn_gate, v_w_ffn_up, v_w_ffn_down]))
    transposed = ("w_in", "w_ffn_gate", "w_ffn_up")

    def view(a, n):
        return a[0].T if n in transposed else a[0]

    def unview(a, n):
        return a.T[None] if n in transposed else a[None]

    delta, new_m, new_v, grads = {}, {}, {}, {}

    def big_adamw(group, gs, name, carry=None):
        (d_, m_, v_), extra = _adamw_multi([view(ws[n], n) for n in group], gs, [view(ms[n], n) for n in group],
                                           [view(vs[n], n) for n in group], name=name, carry=carry)
        for i, n in enumerate(group):
            grads[n], delta[n], new_m[n], new_v[n] = (unview(gs[i], n), unview(d_[i], n), unview(m_[i], n),
                                                      unview(v_[i], n))
        return extra

    in_contribs = big_adamw(["w_ffn_down", "w_ffn_gate", "w_ffn_up", "w_out", "w_branch_hgrn", "w_branch_attn"],
                            [g_dn, g_g, g_u, g_o, g_bh, g_ba], "adamw_first", carry=_carry_chipx(in_pairs))
    in_reds = _rs_chip_add(in_pairs, in_contribs, ci, chip)
    g_in, = [whole(r) for r in _rs_sibling_gather(in_reds)]
    big_adamw(["w_in", "w_ada"], [g_in, g_w_ada], "adamw_second")
    grads.update(c_ctx=g_c_ctx, b_ada=g_b_ada, norm_mix_w=g_nw1, norm_ffn_w=g_nw2, hgrn_lb_logits=g_lg,
                 hgrn_norm_w=g_hw, q_norm_w=g_qnw, k_norm_w=g_knw, attn_sinks=g_sinks)
    small_names = [n for n in names if n not in delta]

    def two_d(a):
        return a.reshape(1, -1) if a.ndim == 1 else a

    sd, sm_, sv = _adamw_small(*[[two_d(d[n]) for n in small_names] for d in (ws, grads, ms, vs)])
    for i, n in enumerate(small_names):
        for dst, src in ((delta, sd), (new_m, sm_), (new_v, sv)):
            dst[n] = src[i].reshape(ws[n].shape)
    return (loss, gx[None], *[grads[n] for n in names], *[delta[n] for n in names],
            *[new_m[n] for n in names], *[new_v[n] for n in names])
```

```python
import functools

import numpy as np
import jax
import jax.numpy as jnp
from jax import lax
from jax.experimental import pallas as pl
from jax.experimental.pallas import tpu as pltpu

F32 = jnp.float32
BF16 = jnp.bfloat16
HI = lax.Precision.HIGHEST
MESH = pl.DeviceIdType.MESH

D = 1024
L = 256
TM = 256
HGW = 512
HGD = 128
CH = 32
ATW = 512
HDIM = 64
BLK = 128
GRID_W = 64
DFF = 2816
NCOL = 5376
EPS = 1e-6
ROPE_THETA = 10000.0

C_FB, C_INP, C_QHG, C_FF = 0, 1, 2, 3
C_GATES = 1
C_GHG, C_QRAW = 8, 9
C_KV = 20
C_QKV = 6

ADAM_LR, ADAM_B1, ADAM_B2, ADAM_EPS, ADAM_WD, ADAM_STEP = 0.001, 0.9, 0.999, 1e-08, 0.01, 10

NN = (((1,), (0,)), ((), ()))
NT = (((1,), (1,)), ((), ()))
TN = (((0,), (0,)), ((), ()))


def _dot(a, b, dims=NN, prec=None):
    return lax.dot_general(a, b, dims, precision=prec, preferred_element_type=F32)


def _bdot(a, b, dims=NN):
    return _dot(a.astype(BF16), b.astype(BF16), dims)


def _sig(x):
    return 1.0 / (1.0 + jnp.exp(-x))


class _Carry:
    def __init__(self, ins, outs, aliases, scratch, phases):
        self.ins, self.outs, self.aliases, self.scratch, self.phases = ins, outs, aliases, scratch, phases


def _in_hbm(args):
    return [pltpu.with_memory_space_constraint(a, pltpu.HBM) for a in args]


def _carry_join(a, b):
    na_in, na_out, na_sc = len(a.ins), len(a.outs), len(a.scratch)
    aliases = dict(a.aliases)
    aliases.update({na_in + i: na_out + o for i, o in b.aliases.items()})

    def phases(ins, outs, sems):
        pa = a.phases(ins[:na_in], outs[:na_out], sems[:na_sc])
        pb = b.phases(ins[na_in:], outs[na_out:], sems[na_sc:])

        def both(fa, fb):
            if fa is None and fb is None:
                return None

            def run():
                for fn in (fa, fb):
                    if fn is not None:
                        fn()
            return run

        return tuple(both(fa, fb) for fa, fb in zip(pa, pb))

    return _Carry(list(a.ins) + list(b.ins), list(a.outs) + list(b.outs), aliases,
                  list(a.scratch) + list(b.scratch), phases)


def _pcall(body, *, name, grid, in_specs, out_specs, out_shape, scratch=(), aliases=None, vmem_mb=48,
           carry=None):
    params = pltpu.CompilerParams(dimension_semantics=("arbitrary",) * len(grid),
                                  vmem_limit_bytes=vmem_mb << 20)
    if carry is None:
        plain = pl.pallas_call(
            body, name=name, grid=grid, in_specs=in_specs, out_specs=out_specs, out_shape=out_shape,
            scratch_shapes=list(scratch), input_output_aliases=aliases or {}, compiler_params=params)
        return lambda *args: plain(*_in_hbm(args))
    single = not isinstance(out_shape, (list, tuple))
    out_specs_l = [out_specs] if single else list(out_specs)
    out_shape_l = [out_shape] if single else list(out_shape)
    n_in, n_out, n_sc = len(in_specs), len(out_shape_l), len(scratch)
    k_in, k_out = len(carry.ins), len(carry.outs)
    nsteps = int(np.prod(grid))
    assert nsteps >= 3

    def wrapped(*refs):
        ins, cins = refs[:n_in], refs[n_in:n_in + k_in]
        o0 = n_in + k_in
        outs, couts = refs[o0:o0 + n_out], refs[o0 + n_out:o0 + n_out + k_out]
        s0 = o0 + n_out + k_out
        sc, csc = refs[s0:s0 + n_sc], refs[s0 + n_sc:]
        step = pl.program_id(0)
        for ax in range(1, len(grid)):
            step = step * grid[ax] + pl.program_id(ax)
        start, mid, end = carry.phases(cins, couts, csc)
        pl.when(step == 0)(start)
        body(*ins, *outs, *sc)
        if mid is not None:
            pl.when(step == nsteps - 2)(mid)
        pl.when(step == nsteps - 1)(end)

    all_aliases = dict(aliases or {})
    all_aliases.update({n_in + i: n_out + o for i, o in carry.aliases.items()})
    call = pl.pallas_call(
        wrapped, name=name, grid=grid, in_specs=list(in_specs) + [ANY] * k_in,
        out_specs=out_specs_l + [ANY] * k_out, out_shape=out_shape_l + list(carry.outs),
        scratch_shapes=list(scratch) + list(carry.scratch), input_output_aliases=all_aliases,
        compiler_params=params)

    def run(*args):
        res = call(*_in_hbm(args), *carry.ins)
        core = res[:n_out]
        return (core[0] if single else list(core)), list(res[n_out:])

    return run


def _full(shape):
    nd = len(shape)
    return pl.BlockSpec(shape, lambda *_: (0,) * nd)


ANY = pl.BlockSpec(memory_space=pl.ANY)


def _mm(a, b, *, name, mode="nn", out_dtype=F32, tm, tn, tk):
    if mode == "nn":
        (m, k), (k2, n) = a.shape, b.shape
    elif mode == "nt":
        (m, k), (n, k2) = a.shape, b.shape
    else:
        (k, m), (k2, n) = a.shape, b.shape
    assert k == k2 and m % tm == 0 and n % tn == 0 and k % tk == 0, (name, a.shape, b.shape)
    nk = k // tk
    dims = {"nn": NN, "nt": NT, "tn": TN}[mode]

    def body(a_ref, b_ref, o_ref, acc):
        kk = pl.program_id(2)

        @pl.when(kk == 0)
        def _():
            acc[...] = jnp.zeros_like(acc)

        acc[...] += _bdot(a_ref[...], b_ref[...], dims)

        @pl.when(kk == nk - 1)
        def _():
            o_ref[...] = acc[...].astype(out_dtype)

    a_spec = (pl.BlockSpec((tk, tm), lambda i, j, kk: (kk, i)) if mode == "tn"
              else pl.BlockSpec((tm, tk), lambda i, j, kk: (i, kk)))
    b_spec = (pl.BlockSpec((tn, tk), lambda i, j, kk: (j, kk)) if mode == "nt"
              else pl.BlockSpec((tk, tn), lambda i, j, kk: (kk, j)))
    return _pcall(body, name=name, grid=(m // tm, n // tn, nk), in_specs=[a_spec, b_spec],
                  out_specs=pl.BlockSpec((tm, tn), lambda i, j, kk: (i, j)),
                  out_shape=jax.ShapeDtypeStruct((m, n), out_dtype),
                  scratch=[pltpu.VMEM((tm, tn), F32)])(a, b)


NT_IN = NCOL // 256


def _src_block(j):
    return j + jnp.where(j < 4, 2, jnp.where(j < 6, 3, jnp.where(j < 8, -6, jnp.where(
        j < 16, 5, jnp.where(j < 20, -7, -14)))))


def _mm_in(h, wt, tm, carry=None):
    tt = h.shape[0]

    def body(h_ref, w_ref, o_ref):
        o_ref[...] = _bdot(h_ref[...], w_ref[...], NT)

    return _pcall(body, name="mm_in", grid=(tt // tm, NT_IN),
                  in_specs=[pl.BlockSpec((tm, D), lambda i, j: (i, 0)),
                            pl.BlockSpec((256, D), lambda i, j: (_src_block(j), 0))],
                  out_specs=pl.BlockSpec((tm, 256), lambda i, j: (i, j)),
                  out_shape=jax.ShapeDtypeStruct((tt, NCOL), F32), carry=carry)(h, wt)


def _mm_dh(dp, wt, tm):
    tt = dp.shape[0]

    def body(d_ref, w_ref, o_ref, acc):
        kk = pl.program_id(1)

        @pl.when(kk == 0)
        def _():
            acc[...] = jnp.zeros_like(acc)

        acc[...] += _bdot(d_ref[...], w_ref[...])

        @pl.when(kk == NT_IN - 1)
        def _():
            o_ref[...] = acc[...]

    return _pcall(body, name="mm_dh", grid=(tt // tm, NT_IN),
                  in_specs=[pl.BlockSpec((tm, 256), lambda i, kk: (i, kk)),
                            pl.BlockSpec((256, D), lambda i, kk: (_src_block(kk), 0))],
                  out_specs=pl.BlockSpec((tm, D), lambda i, kk: (i, 0)),
                  out_shape=jax.ShapeDtypeStruct((tt, D), F32), scratch=[pltpu.VMEM((tm, D), F32)])(dp, wt)


def _mm_gin(dp, h, tk):
    tt = dp.shape[0]
    nk = tt // tk

    def body(d_ref, h_ref, o_ref, acc):
        kk = pl.program_id(1)

        @pl.when(kk == 0)
        def _():
            acc[...] = jnp.zeros_like(acc)

        acc[...] += _bdot(d_ref[...], h_ref[...], TN)

        @pl.when(kk == nk - 1)
        def _():
            o_ref[...] = acc[...]

    return _pcall(body, name="mm_gin", grid=(NT_IN, nk),
                  in_specs=[pl.BlockSpec((tk, 256), lambda j, kk: (kk, j)),
                            pl.BlockSpec((tk, D), lambda j, kk: (kk, 0))],
                  out_specs=pl.BlockSpec((256, D), lambda j, kk: (_src_block(j), 0)),
                  out_shape=jax.ShapeDtypeStruct((NCOL, D), F32), scratch=[pltpu.VMEM((256, D), F32)])(dp, h)


def _modulate(xin, nw, ss, *, name, sel):
    rows = xin.shape[0]

    def body(x_ref, nw_ref, ss_ref, h_ref):
        x = x_ref[...]
        r = lax.rsqrt(jnp.mean(x * x, axis=-1, keepdims=True) + EPS)
        s = ss_ref[0]
        h_ref[...] = ((x * r * nw_ref[...]) * (1.0 + s[1:2]) + s[0:1]).astype(BF16)

    return _pcall(body, name=name, grid=(rows // TM,),
                  in_specs=[pl.BlockSpec((TM, D), lambda i: (i, 0)), _full((1, D)),
                            pl.BlockSpec((1, 2, D), lambda i: (sel(i), 0, 0))],
                  out_specs=pl.BlockSpec((TM, D), lambda i: (i, 0)),
                  out_shape=jax.ShapeDtypeStruct((rows, D), BF16))(xin, nw, ss)


def _norm_bwd_rows(x, dh, nw, scale):
    r = lax.rsqrt(jnp.mean(x * x, axis=-1, keepdims=True) + EPS)
    xh = x * r
    dxh = dh * ((1.0 + scale) * nw)
    dx = r * (dxh - xh * jnp.mean(dxh * xh, axis=-1, keepdims=True))
    return dx, xh


def _res1_mod2(x, ao, g1, nw2, ss2):
    s_len = x.shape[0]

    def body(x_ref, ao_ref, g_ref, nw_ref, ss_ref, x1_ref, h_ref):
        x1 = x_ref[...] + g_ref[...] * ao_ref[...]
        x1_ref[...] = x1
        r = lax.rsqrt(jnp.mean(x1 * x1, axis=-1, keepdims=True) + EPS)
        s = ss_ref[0]
        h_ref[...] = ((x1 * r * nw_ref[...]) * (1.0 + s[1:2]) + s[0:1]).astype(BF16)

    row = pl.BlockSpec((TM, D), lambda i: (i, 0))
    return _pcall(body, name="res1_mod2", grid=(s_len // TM,),
                  in_specs=[row, row, _full((1, D)), _full((1, D)), _full((1, 2, D))],
                  out_specs=[row, row],
                  out_shape=[jax.ShapeDtypeStruct((s_len, D), F32),
                             jax.ShapeDtypeStruct((s_len, D), BF16)])(x, ao, g1, nw2, ss2)


TS = 1024


def _acc_call(body, *, name, grid, in_specs, out_specs, out_shape, acc_shapes, args):
    return _pcall(body, name=name, grid=grid, in_specs=in_specs, out_specs=out_specs, out_shape=out_shape,
                  scratch=[pltpu.VMEM(s, F32) for s in acc_shapes])(*args)


def _mm_cs(a, w4, *, name):
    m, k = a.shape
    _, _, ns = w4.shape

    def body(a_ref, w_ref, o_ref):
        o_ref[...] = _bdot(a_ref[...], w_ref[0])

    return _pcall(body, name=name, grid=(m // TS, 4),
                  in_specs=[pl.BlockSpec((TS, k), lambda i, j: (i, 0)),
                            pl.BlockSpec((1, k, ns), lambda i, j: (j, 0, 0))],
                  out_specs=pl.BlockSpec((TS, ns), lambda i, j: (i, j)),
                  out_shape=jax.ShapeDtypeStruct((m, 4 * ns), F32))(a, w4)


def _mm_cs_nt(a, w4, *, name):
    m = a.shape[0]
    _, k, ns = w4.shape

    def body(a_ref, w_ref, o_ref, acc):
        j = pl.program_id(1)

        @pl.when(j == 0)
        def _():
            acc[...] = jnp.zeros_like(acc)

        acc[...] += _bdot(a_ref[...], w_ref[0], NT)

        @pl.when(j == 3)
        def _():
            o_ref[...] = acc[...]

    return _acc_call(body, name=name, grid=(m // TS, 4),
                     in_specs=[pl.BlockSpec((TS, ns), lambda i, j: (i, j)),
                               pl.BlockSpec((1, k, ns), lambda i, j: (j, 0, 0))],
                     out_specs=pl.BlockSpec((TS, k), lambda i, j: (i, 0)),
                     out_shape=jax.ShapeDtypeStruct((m, k), F32), acc_shapes=[(TS, k)], args=(a, w4))


def _mm_cs_tn(a, b, ns, *, name):
    s_len, k = a.shape
    nk = s_len // TS

    def body(a_ref, b_ref, o_ref, acc):
        t = pl.program_id(1)

        @pl.when(t == 0)
        def _():
            acc[...] = jnp.zeros_like(acc)

        acc[...] += _bdot(a_ref[...], b_ref[...], TN)

        @pl.when(t == nk - 1)
        def _():
            o_ref[0] = acc[...]

    return _acc_call(body, name=name, grid=(4, nk),
                     in_specs=[pl.BlockSpec((TS, k), lambda j, t: (t, 0)),
                               pl.BlockSpec((TS, ns), lambda j, t: (t, j))],
                     out_specs=pl.BlockSpec((1, k, ns), lambda j, t: (j, 0, 0)),
                     out_shape=jax.ShapeDtypeStruct((4, k, ns), F32), acc_shapes=[(k, ns)], args=(a, b))


def _ffn_up(h2, g4, u4):
    s_len = h2.shape[0]
    ns = g4.shape[1]

    def body(h_ref, g_ref, u_ref, a_ref, b_ref, z_ref):
        h = h_ref[...]
        a = _bdot(h, g_ref[0], NT)
        b = _bdot(h, u_ref[0], NT)
        a_ref[0] = a.astype(BF16)
        b_ref[0] = b.astype(BF16)
        z_ref[0] = (a * _sig(a) * b).astype(BF16)

    w = pl.BlockSpec((1, ns, D), lambda i, j: (j, 0, 0))
    o = pl.BlockSpec((1, TS, ns), lambda i, j: (j, i, 0))
    f = jax.ShapeDtypeStruct((4, s_len, ns), BF16)
    return _pcall(body, name="ffn_up", grid=(s_len // TS, 4),
                  in_specs=[pl.BlockSpec((TS, D), lambda i, j: (i, 0)), w, w], out_specs=[o, o, o],
                  out_shape=[f, f, jax.ShapeDtypeStruct((4, s_len, ns), BF16)])(h2, g4, u4)


def _ffn_down(z4, dn4):
    _, s_len, ns = z4.shape

    def body(z_ref, w_ref, o_ref, acc):
        j = pl.program_id(1)

        @pl.when(j == 0)
        def _():
            acc[...] = jnp.zeros_like(acc)

        acc[...] += _bdot(z_ref[0], w_ref[0])

        @pl.when(j == 3)
        def _():
            o_ref[...] = acc[...]

    return _acc_call(body, name="ffn_down", grid=(s_len // TS, 4),
                     in_specs=[pl.BlockSpec((1, TS, ns), lambda i, j: (j, i, 0)),
                               pl.BlockSpec((1, ns, D), lambda i, j: (j, 0, 0))],
                     out_specs=pl.BlockSpec((TS, D), lambda i, j: (i, 0)),
                     out_shape=jax.ShapeDtypeStruct((s_len, D), F32), acc_shapes=[(TS, D)], args=(z4, dn4))


def _ffn_dz(dyb, dn4, a4, b4):
    _, s_len, ns = a4.shape

    def body(dy_ref, w_ref, a_ref, b_ref, da_ref, db_ref):
        dz = _bdot(dy_ref[...], w_ref[0], NT)
        a = a_ref[0].astype(F32)
        s = _sig(a)
        da_ref[0] = (dz * b_ref[0].astype(F32) * (s * (1.0 + a * (1.0 - s)))).astype(BF16)
        db_ref[0] = (dz * (a * s)).astype(BF16)

    t = pl.BlockSpec((1, TS, ns), lambda i, j: (j, i, 0))
    o = jax.ShapeDtypeStruct((4, s_len, ns), BF16)
    return _pcall(body, name="ffn_dz", grid=(s_len // TS, 4),
                  in_specs=[pl.BlockSpec((TS, D), lambda i, j: (i, 0)),
                            pl.BlockSpec((1, ns, D), lambda i, j: (j, 0, 0)), t, t],
                  out_specs=[t, t], out_shape=[o, o])(dyb, dn4, a4, b4)


def _ffn_gdn(z4, dyb):
    _, s_len, ns = z4.shape
    nk = s_len // TS

    def body(z_ref, dy_ref, o_ref, acc):
        t = pl.program_id(1)

        @pl.when(t == 0)
        def _():
            acc[...] = jnp.zeros_like(acc)

        acc[...] += _bdot(z_ref[0], dy_ref[...], TN)

        @pl.when(t == nk - 1)
        def _():
            o_ref[0] = acc[...]

    return _acc_call(body, name="ffn_gdn", grid=(4, nk),
                     in_specs=[pl.BlockSpec((1, TS, ns), lambda j, t: (j, t, 0)),
                               pl.BlockSpec((TS, D), lambda j, t: (t, 0))],
                     out_specs=pl.BlockSpec((1, ns, D), lambda j, t: (j, 0, 0)),
                     out_shape=jax.ShapeDtypeStruct((4, ns, D), F32), acc_shapes=[(ns, D)], args=(z4, dyb))


def _ffn_dh2(da4, db4, g4, u4):
    _, s_len, ns = da4.shape

    def body(da_ref, db_ref, g_ref, u_ref, o_ref, acc):
        j = pl.program_id(1)

        @pl.when(j == 0)
        def _():
            acc[...] = jnp.zeros_like(acc)

        acc[...] += _bdot(da_ref[0], g_ref[0]) + _bdot(db_ref[0], u_ref[0])

        @pl.when(j == 3)
        def _():
            o_ref[...] = acc[...]

    t = pl.BlockSpec((1, TS, ns), lambda i, j: (j, i, 0))
    w = pl.BlockSpec((1, ns, D), lambda i, j: (j, 0, 0))
    return _acc_call(body, name="ffn_dh2", grid=(s_len // TS, 4), in_specs=[t, t, w, w],
                     out_specs=pl.BlockSpec((TS, D), lambda i, j: (i, 0)),
                     out_shape=jax.ShapeDtypeStruct((s_len, D), F32), acc_shapes=[(TS, D)],
                     args=(da4, db4, g4, u4))


def _ffn_ggu(h2, da4, db4):
    _, s_len, ns = da4.shape
    nk = s_len // TS

    def body(h_ref, da_ref, db_ref, gg_ref, gu_ref, acc_g, acc_u):
        t = pl.program_id(1)

        @pl.when(t == 0)
        def _():
            acc_g[...] = jnp.zeros_like(acc_g)
            acc_u[...] = jnp.zeros_like(acc_u)

        h = h_ref[...]
        acc_g[...] += _bdot(da_ref[0], h, TN)
        acc_u[...] += _bdot(db_ref[0], h, TN)

        @pl.when(t == nk - 1)
        def _():
            gg_ref[0] = acc_g[...]
            gu_ref[0] = acc_u[...]

    d = pl.BlockSpec((1, TS, ns), lambda j, t: (j, t, 0))
    o = pl.BlockSpec((1, ns, D), lambda j, t: (j, 0, 0))
    f = jax.ShapeDtypeStruct((4, ns, D), F32)
    return _acc_call(body, name="ffn_ggu", grid=(4, nk),
                     in_specs=[pl.BlockSpec((TS, D), lambda j, t: (t, 0)), d, d], out_specs=[o, o],
                     out_shape=[f, f], acc_shapes=[(ns, D), (ns, D)], args=(h2, da4, db4))


def _loss_head(x1, y, g2, tgt):
    s_len = x1.shape[0]

    def body(x1_ref, y_ref, g_ref, t_ref, sq_ref, dx2_ref, dyb_ref, dg_ref):
        i = pl.program_id(0)

        @pl.when(i == 0)
        def _():
            sq_ref[...] = jnp.zeros_like(sq_ref)
            dg_ref[...] = jnp.zeros_like(dg_ref)

        y_ = y_ref[...]
        g = g_ref[...]
        e = x1_ref[...] + g * y_ - t_ref[...]
        sq_ref[...] += jnp.sum(e * e, axis=0, keepdims=True)
        dx2 = e * (1.0 / D)
        dx2_ref[...] = dx2
        dyb_ref[...] = (g * dx2).astype(BF16)
        dg_ref[...] += jnp.sum(dx2 * y_, axis=0, keepdims=True)

    row = pl.BlockSpec((TM, D), lambda i: (i, 0))
    vec = _full((1, D))
    return _pcall(body, name="loss_head", grid=(s_len // TM,),
                  in_specs=[row, row, vec, row], out_specs=[vec, row, row, vec],
                  out_shape=[jax.ShapeDtypeStruct((1, D), F32), jax.ShapeDtypeStruct((s_len, D), F32),
                             jax.ShapeDtypeStruct((s_len, D), BF16),
                             jax.ShapeDtypeStruct((1, D), F32)])(x1, y, g2, tgt)


def _mod2_bwd(x1, dh2, dx2, ao, nw2, ss2, g1):
    s_len = x1.shape[0]

    def body(x1_ref, dh_ref, dx2_ref, ao_ref, nw_ref, ss_ref, g_ref,
             dx1_ref, da_ref, dss_ref, dnw_ref, dg_ref):
        i = pl.program_id(0)

        @pl.when(i == 0)
        def _():
            dss_ref[...] = jnp.zeros_like(dss_ref)
            dnw_ref[...] = jnp.zeros_like(dnw_ref)
            dg_ref[...] = jnp.zeros_like(dg_ref)

        dh = dh_ref[...]
        nw = nw_ref[...]
        scale = ss_ref[0][1:2]
        dxn, xh = _norm_bwd_rows(x1_ref[...], dh, nw, scale)
        dx1 = dx2_ref[...] + dxn
        dx1_ref[...] = dx1
        da_ref[...] = (g_ref[...] * dx1).astype(BF16)
        dg_ref[...] += jnp.sum(dx1 * ao_ref[...], axis=0, keepdims=True)
        dsh = jnp.sum(dh, axis=0, keepdims=True)
        dsc = jnp.sum(dh * xh * nw, axis=0, keepdims=True)
        dss_ref[...] += jnp.concatenate([dsh, dsc], axis=0)
        dnw_ref[...] += jnp.sum(dh * xh * (1.0 + scale), axis=0, keepdims=True)

    row = pl.BlockSpec((TM, D), lambda i: (i, 0))
    vec = _full((1, D))
    return _pcall(body, name="mod2_bwd", grid=(s_len // TM,),
                  in_specs=[row, row, row, row, vec, _full((1, 2, D)), vec],
                  out_specs=[row, row, _full((2, D)), vec, vec],
                  out_shape=[jax.ShapeDtypeStruct((s_len, D), F32), jax.ShapeDtypeStruct((s_len, D), BF16),
                             jax.ShapeDtypeStruct((2, D), F32), jax.ShapeDtypeStruct((1, D), F32),
                             jax.ShapeDtypeStruct((1, D), F32)])(x1, dh2, dx2, ao, nw2, ss2, g1)


def _mod1_bwd(tok, dh, dx1, nw1, ss1, carry=None):
    tt = tok.shape[0]
    s_len = dx1.shape[0]

    def body(t_ref, dh_ref, dx1_ref, nw_ref, ss_ref, dx_ref, dss_ref, dnw_ref):
        i = pl.program_id(0)

        @pl.when(i == 0)
        def _():
            dnw_ref[...] = jnp.zeros_like(dnw_ref)

        @pl.when(i <= 1)
        def _():
            dss_ref[...] = jnp.zeros_like(dss_ref)

        dh_ = dh_ref[...]
        nw = nw_ref[...]
        scale = ss_ref[0][1:2]
        dxn, xh = _norm_bwd_rows(t_ref[...], dh_, nw, scale)

        @pl.when(i >= 1)
        def _():
            dx_ref[...] = dx1_ref[...] + dxn

        dsh = jnp.sum(dh_, axis=0, keepdims=True)
        dsc = jnp.sum(dh_ * xh * nw, axis=0, keepdims=True)
        dss_ref[...] += jnp.concatenate([dsh, dsc], axis=0)[None]
        dnw_ref[...] += jnp.sum(dh_ * xh * (1.0 + scale), axis=0, keepdims=True)

    row = pl.BlockSpec((TM, D), lambda i: (i, 0))
    lat = pl.BlockSpec((TM, D), lambda i: (jnp.maximum(i - 1, 0), 0))
    sel = pl.BlockSpec((1, 2, D), lambda i: (jnp.minimum(i, 1), 0, 0))
    return _pcall(body, name="mod1_bwd", grid=(tt // TM,),
                  in_specs=[row, row, lat, _full((1, D)), sel],
                  out_specs=[lat, sel, _full((1, D))],
                  out_shape=[jax.ShapeDtypeStruct((s_len, D), F32), jax.ShapeDtypeStruct((2, 2, D), F32),
                             jax.ShapeDtypeStruct((1, D), F32)], carry=carry)(tok, dh, dx1, nw1, ss1)


def _rows(c):
    return slice(c * CH, (c + 1) * CH)


def _chunk_masks(rev, transpose=False):
    r = lax.broadcasted_iota(jnp.int32, (TM, TM), 0)
    c = lax.broadcasted_iota(jnp.int32, (TM, TM), 1)
    same = (r // CH) == (c // CH)
    before = (c >= r) if (rev != transpose) else (c <= r)
    return same & before, same


def _hgrn_gate(fl, qraw, lg):
    lb = 1.0 / (1.0 + jnp.exp(lg[1:2] - lg[0:1]))
    sg = _sig(fl)
    f = lb + (1.0 - lb) * sg
    q = qraw * _sig(qraw) * (HGD ** -0.5)
    return lb, sg, f, q


def _hgrn_fwd(p, lg, *, rev, carry=None):
    tt = p.shape[0]
    nt = tt // TM
    ncht = TM // CH
    d = 1 if rev else 0

    def tile_of(s):
        return jnp.where(s == 0, 0, nt - s) if rev else s

    def body(f_ref, inp_ref, q_ref, lg_ref, o_ref, st_ref, state):
        s = pl.program_id(0)

        @pl.when(s == 0)
        def _():
            state[...] = jnp.zeros_like(state)

        _, _, f, q = _hgrn_gate(f_ref[...], q_ref[...], lg_ref[0])
        lf = jnp.log(f)
        causal, same = _chunk_masks(rev)
        cum = _dot(causal.astype(F32), lf, prec=HI)
        tot = _dot(same.astype(F32), lf, prec=HI)
        qd = (q * jnp.exp(cum)).astype(BF16)
        kd = ((1.0 - f) * jnp.exp(-cum)).astype(BF16)
        ke = ((1.0 - f) * jnp.exp(tot - cum)).astype(BF16)
        et = jnp.exp(tot)
        v = inp_ref[...].astype(BF16)
        order = range(ncht - 1, -1, -1) if rev else range(ncht)
        outs = []
        for h in range(4):
            sl = slice(h * HGD, (h + 1) * HGD)
            qd_, kd_, ke_, v_ = qd[:, sl], kd[:, sl], ke[:, sl], v[:, sl]
            pm = jnp.where(causal, _dot(qd_, kd_, NT), 0.0).astype(BF16)
            o_h = _dot(pm, v_)
            upd = [_dot(v_[_rows(c)], ke_[_rows(c)], TN) for c in range(ncht)]
            st = state[h]
            for c in order:
                st_ref[c, h] = st
                st = st * et[c * CH:c * CH + 1, sl] + upd[c]
            state[h] = st
            inter = [_dot(qd_[_rows(c)], st_ref[c, h].astype(BF16), NT) for c in range(ncht)]
            outs.append(o_h + jnp.concatenate(inter, axis=0))
        o_ref[...] = jnp.concatenate(outs, axis=1)

    def col(cb):
        return pl.BlockSpec((TM, HGW), lambda s: (tile_of(s), cb))

    return _pcall(
        body, name="hgrn_fwd_rev" if rev else "hgrn_fwd", grid=(nt,),
        in_specs=[col(C_FB if rev else C_FF), col(C_INP), col(C_QHG),
                  pl.BlockSpec((1, 2, HGW), lambda s: (d, 0, 0))],
        out_specs=[pl.BlockSpec((TM, HGW), lambda s: (tile_of(s), 0)),
                   pl.BlockSpec((ncht, 4, HGD, HGD), lambda s: (tile_of(s), 0, 0, 0))],
        out_shape=[jax.ShapeDtypeStruct((tt, HGW), F32),
                   jax.ShapeDtypeStruct((nt * ncht, 4, HGD, HGD), F32)],
        scratch=[pltpu.VMEM((4, HGD, HGD), F32)], carry=carry)(p, p, p, lg)


def _hgrn_bwd(p, lg, do, st, dp, prev, *, rev, carry=None):
    tt = p.shape[0]
    nt = tt // TM
    ncht = TM // CH
    d = 1 if rev else 0
    second = prev is not None

    def tile_of(s):
        return jnp.where(s == nt - 1, 0, s + 1) if rev else nt - 1 - s

    def body(*refs):
        if second:
            (f_ref, inp_ref, q_ref, lg_ref, do_ref, st_ref, dvp_ref, dqp_ref, _dp_in,
             dp_ref, dlg_ref, dstate) = refs
        else:
            (f_ref, inp_ref, q_ref, lg_ref, do_ref, st_ref, _dp_in,
             dp_ref, dv_ref, dq_ref, dlg_ref, dstate) = refs
        s = pl.program_id(0)
        tile = tile_of(s)

        @pl.when(s == 0)
        def _():
            dstate[...] = jnp.zeros_like(dstate)
            dlg_ref[...] = jnp.zeros_like(dlg_ref)

        qraw = q_ref[...]
        lb, sg, f, q = _hgrn_gate(f_ref[...], qraw, lg_ref[0])
        lf = jnp.log(f)
        causal, same = _chunk_masks(rev)
        causal_t, _ = _chunk_masks(rev, transpose=True)
        cum = _dot(causal.astype(F32), lf, prec=HI)
        tot = _dot(same.astype(F32), lf, prec=HI)
        ea, eb, ee, et = jnp.exp(cum), jnp.exp(-cum), jnp.exp(tot - cum), jnp.exp(tot)
        qdf, kdf, kef = q * ea, (1.0 - f) * eb, (1.0 - f) * ee
        qd, kd, ke = qdf.astype(BF16), kdf.astype(BF16), kef.astype(BF16)
        v = inp_ref[...].astype(BF16)
        dob = jnp.where(tile == 0, 0.0, do_ref[...]).astype(BF16)
        order = range(ncht) if rev else range(ncht - 1, -1, -1)
        dq_l, dk_l, dv_l, dcum_l, dtot_l = [], [], [], [], []
        for h in range(4):
            sl = slice(h * HGD, (h + 1) * HGD)
            qd_, kd_, ke_, v_, do_ = qd[:, sl], kd[:, sl], ke[:, sl], v[:, sl], dob[:, sl]
            pmt = jnp.where(causal_t, _dot(kd_, qd_, NT), 0.0).astype(BF16)
            dpm = jnp.where(causal, _dot(do_, v_, NT), 0.0).astype(BF16)
            dpmt = jnp.where(causal_t, _dot(v_, do_, NT), 0.0).astype(BF16)
            dv = _dot(pmt, do_)
            dqd = _dot(dpm, kd_)
            dkd = _dot(dpmt, qd_)
            upd = [_dot(do_[_rows(c)], qd_[_rows(c)], TN) for c in range(ncht)]
            ds = dstate[h]
            ds1 = [None] * ncht
            for c in order:
                ds1[c] = ds
                ds = ds * et[c * CH:c * CH + 1, sl] + upd[c]
            dstate[h] = ds
            dke_c, dv_c, dqd_c, dtot_c = [], [], [], []
            for c in range(ncht):
                st0 = st_ref[c, h]
                dsb = ds1[c].astype(BF16)
                dke_ = _dot(v_[_rows(c)], dsb)
                dke_c.append(dke_)
                dv_c.append(_dot(ke_[_rows(c)], dsb, NT))
                dqd_c.append(_dot(do_[_rows(c)], st0.astype(BF16)))
                dt = (jnp.sum(ds1[c] * st0, axis=0, keepdims=True) * et[c * CH:c * CH + 1, sl]
                      + jnp.sum(dke_ * kef[_rows(c), sl], axis=0, keepdims=True))
                dtot_c.append(jnp.broadcast_to(dt, (CH, HGD)))
            dke = jnp.concatenate(dke_c, axis=0)
            dqd = dqd + jnp.concatenate(dqd_c, axis=0)
            dv_l.append(dv + jnp.concatenate(dv_c, axis=0))
            dtot_l.append(jnp.concatenate(dtot_c, axis=0))
            dq_l.append(dqd * ea[:, sl])
            dk_l.append(dkd * eb[:, sl] + dke * ee[:, sl])
            dcum_l.append(dqd * qdf[:, sl] - dkd * kdf[:, sl] - dke * kef[:, sl])
        dcum = jnp.concatenate(dcum_l, axis=1)
        dlf = _dot(causal_t.astype(F32), dcum, prec=HI) + jnp.concatenate(dtot_l, axis=1)
        dq_t = jnp.concatenate(dq_l, axis=1)
        dv_t = jnp.concatenate(dv_l, axis=1)

        df = dlf / f - jnp.concatenate(dk_l, axis=1)
        dfl = df * (1.0 - lb) * sg * (1.0 - sg)
        dlb = jnp.sum(df * (1.0 - sg), axis=0, keepdims=True)
        dl0 = dlb * lb * (1.0 - lb)
        dlg_ref[...] += jnp.concatenate([dl0, -dl0], axis=0)[None]
        if second:
            sq = _sig(qraw)
            dqr = (dqp_ref[...] + dq_t) * (HGD ** -0.5) * (sq * (1.0 + qraw * (1.0 - sq)))
            dp_ref[...] = jnp.concatenate([dfl, dvp_ref[...] + dv_t, dqr], axis=1).astype(BF16)
        else:
            dp_ref[...] = dfl.astype(BF16)
            dv_ref[...] = dv_t
            dq_ref[...] = dq_t

    def col(cb):
        return pl.BlockSpec((TM, HGW), lambda s: (tile_of(s), cb))

    tok = pl.BlockSpec((TM, HGW), lambda s: (tile_of(s), 0))
    in_specs = [col(C_FB if rev else C_FF), col(C_INP), col(C_QHG),
                pl.BlockSpec((1, 2, HGW), lambda s: (d, 0, 0)),
                pl.BlockSpec((TM, HGW), lambda s: (jnp.maximum(tile_of(s) - 1, 0), 0)),
                pl.BlockSpec((ncht, 4, HGD, HGD), lambda s: (tile_of(s), 0, 0, 0))]
    args = [p, p, p, lg, do, st]
    dlg_spec = _full((1, 2, HGW))
    dlg_shape = jax.ShapeDtypeStruct((1, 2, HGW), F32)
    if second:
        in_specs += [tok, tok]
        args += [prev[0], prev[1]]
        out_specs = [pl.BlockSpec((TM, 3 * HGW), lambda s: (tile_of(s), 0)), dlg_spec]
        out_shape = [jax.ShapeDtypeStruct(dp.shape, BF16), dlg_shape]
    else:
        out_specs = [pl.BlockSpec((TM, HGW), lambda s: (tile_of(s), C_FB if rev else C_FF)), tok, tok, dlg_spec]
        out_shape = [jax.ShapeDtypeStruct(dp.shape, BF16), jax.ShapeDtypeStruct((tt, HGW), F32),
                     jax.ShapeDtypeStruct((tt, HGW), F32), dlg_shape]
    in_specs.append(ANY)
    args.append(dp)
    return _pcall(body, name="hgrn_bwd_rev" if rev else "hgrn_bwd", grid=(nt,),
                  in_specs=in_specs, out_specs=out_specs, out_shape=out_shape,
                  scratch=[pltpu.VMEM((4, HGD, HGD), F32)],
                  aliases={len(args) - 1: 0}, carry=carry)(*args)


def _head_rms(o, w, nheads):
    outs = []
    for h in range(nheads):
        oh = o[:, h * HGD:(h + 1) * HGD]
        outs.append(oh * lax.rsqrt(jnp.mean(oh * oh, axis=-1, keepdims=True) + EPS))
    return jnp.concatenate(outs, axis=1)


def _readout(o0, o1, p, hw4):
    s_len = o0.shape[0] - L

    def body(o0_ref, o1_ref, g_ref, w_ref, y_ref):
        xh = _head_rms(o0_ref[...] + o1_ref[...], None, 4)
        g = g_ref[...]
        y_ref[...] = (xh * w_ref[...] * (g * _sig(g))).astype(BF16)

    lat = pl.BlockSpec((TM, HGW), lambda i: (i + 1, 0))
    return _pcall(body, name="readout", grid=(s_len // TM,),
                  in_specs=[lat, lat, pl.BlockSpec((TM, HGW), lambda i: (i + 1, C_GHG)), _full((1, HGW))],
                  out_specs=pl.BlockSpec((TM, HGW), lambda i: (i, 0)),
                  out_shape=jax.ShapeDtypeStruct((s_len, HGW), BF16))(o0, o1, p, hw4)


def _readout_bwd(o0, o1, p, hw4, dy, dp, carry=None):
    tt = o0.shape[0]
    s_len = tt - L

    def body(o0_ref, o1_ref, g_ref, w_ref, dy_ref, _dp_in, dp_ref, do_ref, dw_ref):
        i = pl.program_id(0)

        @pl.when(i == 0)
        def _():
            dw_ref[...] = jnp.zeros_like(dw_ref)
            dp_ref[...] = jnp.zeros_like(dp_ref)

        @pl.when(i >= 1)
        def _():
            o = o0_ref[...] + o1_ref[...]
            g = g_ref[...]
            w = w_ref[...]
            sg = _sig(g)
            dy_ = dy_ref[...]
            dsw = dy_ * (g * sg)
            outs, xhs = [], []
            for h in range(4):
                sl = slice(h * HGD, (h + 1) * HGD)
                oh = o[:, sl]
                r = lax.rsqrt(jnp.mean(oh * oh, axis=-1, keepdims=True) + EPS)
                xh = oh * r
                dxh = dsw[:, sl] * w[:, sl]
                outs.append(r * (dxh - xh * jnp.mean(dxh * xh, axis=-1, keepdims=True)))
                xhs.append(xh)
            xh = jnp.concatenate(xhs, axis=1)
            do_ref[...] = jnp.concatenate(outs, axis=1)
            dp_ref[...] = (dy_ * xh * w * (sg * (1.0 + g * (1.0 - sg)))).astype(BF16)
            dw_ref[...] += jnp.sum(dsw * xh, axis=0, keepdims=True)

    tok = pl.BlockSpec((TM, HGW), lambda i: (i, 0))
    lat = pl.BlockSpec((TM, HGW), lambda i: (jnp.maximum(i - 1, 0), 0))
    return _pcall(body, name="readout_bwd", grid=(tt // TM,),
                  in_specs=[tok, tok, pl.BlockSpec((TM, HGW), lambda i: (i, C_GHG)), _full((1, HGW)), lat, ANY],
                  out_specs=[pl.BlockSpec((TM, HGW), lambda i: (i, C_GHG)), lat, _full((1, HGW))],
                  out_shape=[jax.ShapeDtypeStruct(dp.shape, BF16), jax.ShapeDtypeStruct((s_len, HGW), F32),
                             jax.ShapeDtypeStruct((1, HGW), F32)],
                  aliases={5: 0}, carry=carry)(o0, o1, p, hw4, dy, dp)


def _rope_tables(s_len):
    t = np.arange(s_len)
    inv = ROPE_THETA ** (-np.arange(0, 32, 2, dtype=np.float64) / 32)
    def half(pos):
        ang = pos[:, None].astype(np.float64) * inv[None, :]
        return (np.concatenate([np.cos(ang), np.cos(ang)], 1), np.concatenate([-np.sin(ang), np.sin(ang)], 1))
    cr, sr = half(t // GRID_W)
    cc, sc = half(t % GRID_W)
    cos = np.concatenate([cr, cc, cr, cc], 1)
    sin = np.concatenate([sr, sc, sr, sc], 1)
    cos = np.concatenate([np.ones((L, 128)), cos], 0)
    sin = np.concatenate([np.zeros((L, 128)), sin], 0)
    return jnp.asarray(cos, F32), jnp.asarray(sin, F32)


def _blockdiag(n, w):
    i = np.arange(n)
    return jnp.asarray((i[:, None] // w == i[None, :] // w) / float(w), F32)


def _dup_matrix():
    m = np.zeros((128, 512), np.float32)
    for g in range(2):
        for j in range(4):
            for dd in range(HDIM):
                m[64 * g + dd, 256 * g + 64 * j + dd] = 1.0
    return m


def _rot(x):
    n = x.shape[1]
    lane = lax.broadcasted_iota(jnp.int32, x.shape, 1)
    return jnp.where((lane % 32) < 16, pltpu.roll(x, n - 16, 1), pltpu.roll(x, 16, 1))


def _qk_prep(p, cos, sin, qnw8, knw2, bd512, bd128, dup):
    tt = p.shape[0]

    def body(q_ref, kv_ref, cos_ref, sin_ref, qw_ref, kw_ref, b5_ref, b1_ref, dup_ref,
             qr_ref, k4_ref, v4_ref):
        cos_, sin_ = cos_ref[...], sin_ref[...]
        q = q_ref[...]
        qn = q * lax.rsqrt(_dot(q * q, b5_ref[...], prec=HI) + EPS) * qw_ref[...]
        cos4 = jnp.concatenate([cos_] * 4, axis=1)
        sin4 = jnp.concatenate([sin_] * 4, axis=1)
        qr_ref[...] = ((qn * cos4 + _rot(qn) * sin4) * (HDIM ** -0.5)).astype(BF16)
        kv = kv_ref[...]
        k, v = kv[:, :128], kv[:, 128:]
        kn = k * lax.rsqrt(_dot(k * k, b1_ref[...], prec=HI) + EPS) * kw_ref[...]
        kr = kn * cos_ + _rot(kn) * sin_
        k4_ref[...] = _bdot(kr, dup_ref[...]).astype(BF16)
        v4_ref[...] = _bdot(v, dup_ref[...]).astype(BF16)

    row = lambda w, cb: pl.BlockSpec((TM, w), lambda i: (i, cb))
    out = jax.ShapeDtypeStruct((tt, ATW), BF16)
    return _pcall(body, name="qk_prep", grid=(tt // TM,),
                  in_specs=[row(ATW, C_QRAW), row(256, C_KV), row(128, 0), row(128, 0),
                            _full((1, ATW)), _full((1, 128)), _full((ATW, ATW)), _full((128, 128)),
                            _full((128, ATW))],
                  out_specs=[row(ATW, 0)] * 3, out_shape=[out] * 3)(
                      p, p, cos, sin, qnw8, knw2, bd512, bd128, dup)


def _attn_masks(i, nb):
    r = lax.broadcasted_iota(jnp.int32, (BLK, 3 * BLK + L), 0)
    c = lax.broadcasted_iota(jnp.int32, (BLK, 3 * BLK + L), 1)
    kpos = (i - 1) * BLK + c
    loc = (jnp.abs(c - BLK - r) <= BLK) & (kpos >= 0) & (kpos < nb * BLK)
    return loc | (c >= 3 * BLK)


def _lane_mask(j):
    lane = lax.broadcasted_iota(jnp.int32, (1, 256), 1)
    return (lane // HDIM) == j


def _attn_specs(nb):
    blk = lambda off: pl.BlockSpec((BLK, ATW), lambda i: (jnp.clip(i + off, 0, nb - 1) + 2, 0))
    ctx = pl.BlockSpec((L, ATW), lambda i: (0, 0))
    return blk, ctx


def _attn_fwd(qr, k4, v4, sinks, carry=None):
    tt = qr.shape[0]
    s_len = tt - L
    nb = s_len // BLK

    def body(sk_ref, q_ref, kp, ko, kn, kc, vp, vo, vn, vc, y_ref, lse_ref):
        i = pl.program_id(0)
        valid = _attn_masks(i, nb)
        q = q_ref[...]
        ys, lses = [], []
        for g in range(2):
            gs = slice(256 * g, 256 * g + 256)
            kcat = jnp.concatenate([kp[:, gs], ko[:, gs], kn[:, gs], kc[:, gs]], axis=0)
            vcat = jnp.concatenate([vp[:, gs], vo[:, gs], vn[:, gs], vc[:, gs]], axis=0)
            qg = q[:, gs]
            og = jnp.zeros((BLK, 256), F32)
            lg = jnp.zeros((BLK, 256), F32)
            for j in range(4):
                lm = _lane_mask(j)
                sink = sk_ref[4 * g + j]
                s = jnp.where(valid, _dot(jnp.where(lm, qg, jnp.zeros_like(qg)), kcat, NT), -1e30)
                m = jnp.maximum(jnp.max(s, axis=-1, keepdims=True), sink)
                e = jnp.exp(s - m)
                den = jnp.sum(e, axis=-1, keepdims=True) + jnp.exp(sink - m)
                pr = e / den
                og = og + jnp.where(lm, _bdot(pr, vcat), 0.0)
                lg = lg + jnp.where(lm, m + jnp.log(den), 0.0)
            ys.append(og)
            lses.append(lg)
        y_ref[...] = jnp.concatenate(ys, axis=1).astype(BF16)
        lse_ref[...] = jnp.concatenate(lses, axis=1)

    blk, ctx = _attn_specs(nb)
    out = pl.BlockSpec((BLK, ATW), lambda i: (i, 0))
    return _pcall(body, name="attn_fwd", grid=(nb,),
                  in_specs=[pl.BlockSpec(memory_space=pltpu.SMEM), blk(0),
                            blk(-1), blk(0), blk(1), ctx, blk(-1), blk(0), blk(1), ctx],
                  out_specs=[out, out],
                  out_shape=[jax.ShapeDtypeStruct((s_len, ATW), BF16),
                             jax.ShapeDtypeStruct((s_len, ATW), F32)], carry=carry)(
                      sinks, qr, k4, k4, k4, k4, v4, v4, v4, v4)


def _attn_bwd(qr, k4, v4, sinks, y, lse, dy, carry=None):
    tt = qr.shape[0]
    s_len = tt - L
    nb = s_len // BLK

    def body(sk_ref, q_ref, kp, ko, kn, kc, vp, vo, vn, vc, y_ref, lse_ref, dy_ref,
             dq_ref, dkw_ref, dvw_ref, dkc_ref, dvc_ref, dsk_ref):
        i = pl.program_id(0)

        @pl.when(i == 0)
        def _():
            dkc_ref[...] = jnp.zeros_like(dkc_ref)
            dvc_ref[...] = jnp.zeros_like(dvc_ref)
            dsk_ref[...] = jnp.zeros_like(dsk_ref)

        valid = _attn_masks(i, nb)
        q = q_ref[...]
        dy_ = dy_ref[...]
        dly = dy_ * y_ref[...].astype(F32)
        lse_ = lse_ref[...]
        dqs = []
        for g in range(2):
            gs = slice(256 * g, 256 * g + 256)
            kcat = jnp.concatenate([kp[:, gs], ko[:, gs], kn[:, gs], kc[:, gs]], axis=0)
            vcat = jnp.concatenate([vp[:, gs], vo[:, gs], vn[:, gs], vc[:, gs]], axis=0)
            qg, dyg, dlg, lsg = q[:, gs], dy_[:, gs], dly[:, gs], lse_[:, gs]
            dqg = jnp.zeros((BLK, 256), F32)
            dkg = jnp.zeros((3 * BLK + L, 256), F32)
            dvg = jnp.zeros((3 * BLK + L, 256), F32)
            for j in range(4):
                lm = _lane_mask(j)
                sink = sk_ref[4 * g + j]
                qm = jnp.where(lm, qg, jnp.zeros_like(qg))
                dym = jnp.where(lm, dyg, 0.0).astype(BF16)
                lse_h = jnp.max(jnp.where(lm, lsg, -1e30), axis=-1, keepdims=True)
                delta = jnp.sum(jnp.where(lm, dlg, 0.0), axis=-1, keepdims=True)
                s = _dot(qm, kcat, NT)
                pr = jnp.where(valid, jnp.exp(s - lse_h), 0.0)
                dpr = _dot(dym, vcat, NT)
                dsc = pr * (dpr - delta)
                psink = jnp.exp(sink - lse_h)
                dsk_ref[4 * g + j:4 * g + j + 1, :] += jnp.broadcast_to(
                    -jnp.sum(psink * delta, axis=0, keepdims=True), (1, 128))
                dsb = dsc.astype(BF16)
                dqg = dqg + jnp.where(lm, _dot(dsb, kcat), 0.0)
                dkg = dkg + _dot(dsb, qm, TN)
                dvg = dvg + _dot(pr.astype(BF16), dym, TN)
            dqs.append(dqg)
            dkw_ref[0, :, gs] = dkg[:3 * BLK]
            dvw_ref[0, :, gs] = dvg[:3 * BLK]
            dkc_ref[:, gs] += dkg[3 * BLK:]
            dvc_ref[:, gs] += dvg[3 * BLK:]
        dq_ref[...] = jnp.concatenate(dqs, axis=1)

    blk, ctx = _attn_specs(nb)
    out = pl.BlockSpec((BLK, ATW), lambda i: (i, 0))
    win = pl.BlockSpec((1, 3 * BLK, ATW), lambda i: (i, 0, 0))
    acc = _full((L, ATW))
    return _pcall(body, name="attn_bwd", grid=(nb,),
                  in_specs=[pl.BlockSpec(memory_space=pltpu.SMEM), blk(0),
                            blk(-1), blk(0), blk(1), ctx, blk(-1), blk(0), blk(1), ctx, out, out, out],
                  out_specs=[out, win, win, acc, acc, _full((8, 128))],
                  out_shape=[jax.ShapeDtypeStruct((s_len, ATW), F32),
                             jax.ShapeDtypeStruct((nb, 3 * BLK, ATW), F32),
                             jax.ShapeDtypeStruct((nb, 3 * BLK, ATW), F32),
                             jax.ShapeDtypeStruct((L, ATW), F32), jax.ShapeDtypeStruct((L, ATW), F32),
                             jax.ShapeDtypeStruct((8, 128), F32)], carry=carry)(
                      sinks, qr, k4, k4, k4, k4, v4, v4, v4, v4, y, lse, dy)


def _attn_post(p, cos, sin, qnw8, knw2, bd512, bd128, dupt, dq, dkw, dvw, dkc, dvc, dp, carry=None):
    tt = p.shape[0]
    s_len = tt - L
    nb = s_len // BLK
    nctx = L // BLK

    def body(q_ref, kv_ref, cos_ref, sin_ref, qw_ref, kw_ref, b5_ref, b1_ref, dupt_ref,
             dq_ref, kwp, kwo, kwn, vwp, vwo, vwn, dkc_ref, dvc_ref, _dp_in,
             dp_ref, dqw_ref, dkw_ref):
        t = pl.program_id(0)
        j = t - nctx

        @pl.when(t == 0)
        def _():
            dqw_ref[...] = jnp.zeros_like(dqw_ref)
            dkw_ref[...] = jnp.zeros_like(dkw_ref)

        is_lat = t >= nctx
        cos_, sin_ = cos_ref[...], sin_ref[...]
        has_p = is_lat & (j >= 1)
        has_n = is_lat & (j <= nb - 2)
        dk4 = (jnp.where(is_lat, kwo[0], dkc_ref[...]) + jnp.where(has_p, kwp[0], 0.0)
               + jnp.where(has_n, kwn[0], 0.0))
        dv4 = (jnp.where(is_lat, vwo[0], dvc_ref[...]) + jnp.where(has_p, vwp[0], 0.0)
               + jnp.where(has_n, vwn[0], 0.0))
        dkr = _dot(dk4, dupt_ref[...], prec=HI)
        dv = _dot(dv4, dupt_ref[...], prec=HI)
        kv = kv_ref[...]
        k = kv[:, :128]
        kw = kw_ref[...]
        rk = lax.rsqrt(_dot(k * k, b1_ref[...], prec=HI) + EPS)
        xk = k * rk
        dkn = dkr * cos_ + _rot(dkr * sin_)
        dxk = dkn * kw
        dk = rk * (dxk - xk * _dot(dxk * xk, b1_ref[...], prec=HI))
        dkw_ref[...] += jnp.sum(dkn * xk, axis=0, keepdims=True)
        q = q_ref[...]
        qw = qw_ref[...]
        rq = lax.rsqrt(_dot(q * q, b5_ref[...], prec=HI) + EPS)
        xq = q * rq
        cos4 = jnp.concatenate([cos_] * 4, axis=1)
        sin4 = jnp.concatenate([sin_] * 4, axis=1)
        dqr = jnp.where(is_lat, dq_ref[...], 0.0) * (HDIM ** -0.5)
        dqn = dqr * cos4 + _rot(dqr * sin4)
        dxq = dqn * qw
        dqraw = rq * (dxq - xq * _dot(dxq * xq, b5_ref[...], prec=HI))
        dqw_ref[...] += jnp.sum(dqn * xq, axis=0, keepdims=True)
        dp_ref[...] = jnp.concatenate([dqraw, dk, dv], axis=1).astype(BF16)

    row = lambda w, cb: pl.BlockSpec((BLK, w), lambda t: (t, cb))
    lat = pl.BlockSpec((BLK, ATW), lambda t: (jnp.maximum(t - nctx, 0), 0))

    def part(off):
        return pl.BlockSpec((1, BLK, ATW), lambda t: (jnp.clip(t - nctx + off, 0, nb - 1), 1 - off, 0))

    cacc = pl.BlockSpec((BLK, ATW), lambda t: (jnp.minimum(t, nctx - 1), 0))
    return _pcall(body, name="attn_post", grid=(tt // BLK,),
                  in_specs=[row(ATW, C_QRAW), row(256, C_KV), row(128, 0), row(128, 0),
                            _full((1, ATW)), _full((1, 128)), _full((ATW, ATW)), _full((128, 128)),
                            _full((ATW, 128)), lat, part(-1), part(0), part(1), part(-1), part(0), part(1),
                            cacc, cacc, ANY],
                  out_specs=[pl.BlockSpec((BLK, 768), lambda t: (t, C_QKV)), _full((1, ATW)), _full((1, 128))],
                  out_shape=[jax.ShapeDtypeStruct(dp.shape, BF16), jax.ShapeDtypeStruct((1, ATW), F32),
                             jax.ShapeDtypeStruct((1, 128), F32)],
                  aliases={18: 0}, carry=carry)(p, p, cos, sin, qnw8, knw2, bd512, bd128, dupt,
                                   dq, dkw, dkw, dkw, dvw, dvw, dvw, dkc, dvc, dp)


def _merge(ah, aa, p):
    s_len = ah.shape[0]

    def body(ah_ref, aa_ref, gh_ref, ga_ref, m_ref):
        m_ref[...] = (_sig(gh_ref[...]) * ah_ref[...] + _sig(ga_ref[...]) * aa_ref[...]).astype(BF16)

    row = pl.BlockSpec((TM, D), lambda i: (i, 0))
    return _pcall(body, name="merge", grid=(s_len // TM,),
                  in_specs=[row, row, pl.BlockSpec((TM, D), lambda i: (i + 1, 2)),
                            pl.BlockSpec((TM, D), lambda i: (i + 1, 3))],
                  out_specs=row, out_shape=jax.ShapeDtypeStruct((s_len, D), BF16))(ah, aa, p, p)


def _merge_bwd(dm, ah, aa, p, carry=None):
    tt = p.shape[0]
    s_len = tt - L

    def body(dm_ref, ah_ref, aa_ref, gh_ref, ga_ref, dp_ref, dmh_ref, dma_ref):
        i = pl.program_id(0)

        @pl.when(i == 0)
        def _():
            dp_ref[...] = jnp.zeros_like(dp_ref)

        @pl.when(i >= 1)
        def _():
            dm_ = dm_ref[...]
            sh, sa = _sig(gh_ref[...]), _sig(ga_ref[...])
            dp_ref[...] = jnp.concatenate([dm_ * ah_ref[...] * sh * (1.0 - sh),
                                           dm_ * aa_ref[...] * sa * (1.0 - sa)], axis=1).astype(BF16)
            dmh_ref[...] = (dm_ * sh).astype(BF16)
            dma_ref[...] = (dm_ * sa).astype(BF16)

    lat = pl.BlockSpec((TM, D), lambda i: (jnp.maximum(i - 1, 0), 0))
    return _pcall(body, name="merge_bwd", grid=(tt // TM,),
                  in_specs=[lat, lat, lat, pl.BlockSpec((TM, D), lambda i: (i, 2)),
                            pl.BlockSpec((TM, D), lambda i: (i, 3))],
                  out_specs=[pl.BlockSpec((TM, 2 * D), lambda i: (i, C_GATES)), lat, lat],
                  out_shape=[jax.ShapeDtypeStruct((tt, NCOL), BF16), jax.ShapeDtypeStruct((s_len, D), BF16),
                             jax.ShapeDtypeStruct((s_len, D), BF16)], carry=carry)(dm, ah, aa, p, p)


def _local_step(x, ctx, tgt, mod, modc, nw1, nw2, lg, hw, qnw, knw, sinks,
                w_in, wts, dist=None):
    s_len = x.shape[0]
    tt = s_len + L
    tok = jnp.concatenate([ctx, x], axis=0)
    ss1 = jnp.stack([modc, mod[0:2]])
    ss2 = mod[3:5][None]
    g1, g2 = mod[2:3], mod[5:6]
    hw4 = jnp.tile(hw, (1, 4))
    qnw8 = jnp.tile(qnw, (1, 8))
    knw2 = jnp.tile(knw, (1, 2))
    cos, sin = _rope_tables(s_len)
    bd512, bd128 = _blockdiag(ATW, HDIM), _blockdiag(128, HDIM)
    dupm = _dup_matrix()
    dup, dupt = jnp.asarray(dupm, BF16), jnp.asarray(dupm.T, F32)
    tmt = tt

    def four(b):
        return b.reshape(4, 2 * b.shape[1], b.shape[2])

    def halves(g):
        return g.reshape(4, 2, g.shape[1] // 2, g.shape[2])

    h = _modulate(tok, nw1, ss1, name="mod1", sel=lambda i: jnp.minimum(i, 1))
    if dist is None:
        bh4, ba4, w_o, g4, u4, dn4 = wts
        p = _mm_in(h, w_in, tmt)
        o0, st0 = _hgrn_fwd(p, lg, rev=False)
        o1, st1 = _hgrn_fwd(p, lg, rev=True)
    else:
        core, chip = dist
        p, first = _mm_in(h, w_in, tmt, carry=_carry_gather(list(wts[0:3])))
        (o0, st0), (g8,) = _hgrn_fwd(p, lg, rev=False, carry=_carry_gather([wts[3]]))
        o1, st1 = _hgrn_fwd(p, lg, rev=True)
        bh4, ba4, w_o, g4 = four(first[0]), four(first[1]), four(first[2]).reshape(D, D), four(g8)
    y_hg = _readout(o0, o1, p, hw4)
    qr, k4, v4 = _qk_prep(p, cos, sin, qnw8, knw2, bd512, bd128, dup)
    if dist is None:
        y_at, lse = _attn_fwd(qr, k4, v4, sinks)
    else:
        (y_at, lse), (u8, dn8) = _attn_fwd(qr, k4, v4, sinks, carry=_carry_gather(list(wts[4:6])))
        u4, dn4 = four(u8), four(dn8)
    ah = _mm_cs(y_hg, bh4, name="mm_bh")
    aa = _mm_cs(y_at, ba4, name="mm_ba")
    mixed = _merge(ah, aa, p)
    ao = _mm(mixed, w_o, name="mm_o", tm=512, tn=D, tk=D)
    x1, h2 = _res1_mod2(x, ao, g1, nw2, ss2)
    a4, b4, z4 = _ffn_up(h2, g4, u4)
    y = _ffn_down(z4, dn4)
    sq, dx2, dyb, dg2 = _loss_head(x1, y, g2, tgt)

    da4, db4 = _ffn_dz(dyb, dn4, a4, b4)
    g_dn = _ffn_gdn(z4, dyb)
    dh2 = _ffn_dh2(da4, db4, g4, u4)
    g_g, g_u = _ffn_ggu(h2, da4, db4)
    dx1, dattn, dss2, dnw2, dg1 = _mod2_bwd(x1, dh2, dx2, ao, nw2, ss2, g1)
    dm = _mm(dattn, w_o, name="mm_dm", mode="nt", tm=512, tn=D, tk=D)
    g_o = _mm(mixed, dattn, name="mm_go", mode="tn", tm=D, tn=D, tk=512)
    if dist is None:
        dp, dmh, dma = _merge_bwd(dm, ah, aa, p)
    else:
        ffn_units = [halves(g_dn), halves(g_g), halves(g_u)]
        (dp, dmh, dma), ffn_recv = _merge_bwd(dm, ah, aa, p, carry=_carry_pairx(ffn_units))
        ffn_pairs = _rs_pair_add(ffn_units, ffn_recv, core)
    dy_hg = _mm_cs_nt(dmh, bh4, name="mm_dyh")
    dy_at = _mm_cs_nt(dma, ba4, name="mm_dya")
    g_bh = _mm_cs_tn(y_hg, dmh, D // 4, name="mm_gbh")
    g_ba = _mm_cs_tn(y_at, dma, D // 4, name="mm_gba")
    if dist is None:
        dp, do, dhw4 = _readout_bwd(o0, o1, p, hw4, dy_hg, dp)
        dq, dkw, dvw, dkc, dvc, dsk = _attn_bwd(qr, k4, v4, sinks, y_at, lse, dy_at)
        dp, dqnw8, dknw2 = _attn_post(p, cos, sin, qnw8, knw2, bd512, bd128, dupt, dq, dkw, dvw, dkc, dvc, dp)
    else:
        mix_units = [halves(g_bh), halves(g_ba), halves(g_o.reshape(4, D // 4, D))]
        (dp, do, dhw4), mix_recv = _readout_bwd(o0, o1, p, hw4, dy_hg, dp, carry=_carry_pairx(mix_units))
        mix_pairs = _rs_pair_add(mix_units, mix_recv, core)
        (dq, dkw, dvw, dkc, dvc, dsk), contribs_a = _attn_bwd(qr, k4, v4, sinks, y_at, lse, dy_at,
                                                              carry=_carry_chipx(ffn_pairs[0:2]))
        reds_a = _rs_chip_add(ffn_pairs[0:2], contribs_a, core, chip)
        (dp, dqnw8, dknw2), post = _attn_post(
            p, cos, sin, qnw8, knw2, bd512, bd128, dupt, dq, dkw, dvw, dkc, dvc, dp,
            carry=_carry_join(_carry_chipx(ffn_pairs[2:3]), _carry_sibx(reds_a)))
        reds_b = _rs_chip_add(ffn_pairs[2:3], post[0:1], core, chip)
    if dist is None:
        dp, dv0, dq0, dlg0 = _hgrn_bwd(p, lg, do, st0, dp, None, rev=False)
        dp, dlg1 = _hgrn_bwd(p, lg, do, st1, dp, (dv0, dq0), rev=True)
    else:
        (dp, dv0, dq0, dlg0), mid = _hgrn_bwd(p, lg, do, st0, dp, None, rev=False,
                                              carry=_carry_join(_carry_chipx(mix_pairs), _carry_sibx(reds_b)))
        mix_reds = _rs_chip_add(mix_pairs, mid[0:3], core, chip)
        (dp, dlg1), mix_done = _hgrn_bwd(p, lg, do, st1, dp, (dv0, dq0), rev=True, carry=_carry_sibx(mix_reds))
        ffn_done = post[1:3] + mid[3:4]
    dh = _mm_dh(dp, w_in, tmt)
    g_in = _mm_gin(dp, h, tmt)
    if dist is None:
        gx, dss1, dnw1 = _mod1_bwd(tok, dh, dx1, nw1, ss1)
        rs = None
    else:
        in_units = [halves(g_in.reshape(4, NCOL // 4, D))]
        (gx, dss1, dnw1), in_recv = _mod1_bwd(tok, dh, dx1, nw1, ss1, carry=_carry_pairx(in_units))
        rs = dict(ffn_done=ffn_done, mix_done=mix_done, in_units=in_units, in_recv=in_recv)

    dmod = jnp.concatenate([dss1[1], dg1, dss2, dg2], axis=0)
    dmodc = dss1[0]
    raw = (dss1, dg1, dss2, dg2, dnw1, dnw2, dhw4, dqnw8, dknw2, dsk, dlg0, dlg1)
    small = dict(raw=raw, dmod=dmod, dmodc=dmodc, dnw1=dnw1, dnw2=dnw2,
                 dhw=dhw4.reshape(4, HGD).sum(0, keepdims=True),
                 dqnw=dqnw8.reshape(8, HDIM).sum(0, keepdims=True),
                 dknw=dknw2.reshape(2, HDIM).sum(0, keepdims=True),
                 dsinks=dsk[:, 0], dlg=jnp.concatenate([dlg0, dlg1], axis=0))
    big = dict(w_in=g_in, w_bh=g_bh, w_ba=g_ba, w_o=g_o, w_g=g_g, w_u=g_u, w_dn=g_dn)
    return sq, gx, big, small, rs


def _place():
    x, y, c = lax.axis_index("x"), lax.axis_index("y"), lax.axis_index("c")
    return x, y, c


def _gather_blocks(x_refs, out_refs, send_sems, recv_sems, local_sems):
    n = len(out_refs)
    x, y, c = _place()
    me, sibling = (x, y, c), (x, y, 1 - c)
    chips = [(1 - x, y), (x, 1 - y), (1 - x, 1 - y)]

    def slot(u, px, py, pc):
        return out_refs[u].at[4 * px + 2 * py + pc]

    def copy(u, k, block, to, src=None):
        return pltpu.make_async_remote_copy(
            src_ref=slot(u, *block) if src is None else src, dst_ref=slot(u, *block),
            send_sem=send_sems.at[u, k], recv_sem=recv_sems.at[u, k], device_id=to, device_id_type=MESH)

    mines = []
    if x_refs is not None:
        mines = [pltpu.make_async_copy(x_refs[u], slot(u, *me), local_sems.at[u]) for u in range(n)]
    for cp in mines:
        cp.start()
    first = []
    for u in range(n):
        src = None if x_refs is None else x_refs[u]
        first.append(copy(u, 0, me, sibling, src=src))
        first += [copy(u, 1 + j, me, (*chip, c), src=src) for j, chip in enumerate(chips)]
    for cp in first:
        cp.start()
    passed = []
    for j, chip in enumerate(chips):
        for u in range(n):
            copy(u, 1 + j, (*chip, c), me).wait_recv()
            fwd = copy(u, 4 + j, (*chip, c), sibling)
            fwd.start()
            passed.append(fwd)
    for u in range(n):
        copy(u, 0, sibling, me).wait_recv()
    for j, chip in enumerate(chips):
        for u in range(n):
            copy(u, 4 + j, (*chip, 1 - c), me).wait_recv()
    for cp in first + passed:
        cp.wait_send()
    for cp in mines:
        cp.wait()


def _gather_sems(n):
    return [pltpu.SemaphoreType.DMA((n, 7)), pltpu.SemaphoreType.DMA((n, 7)), pltpu.SemaphoreType.DMA((n,))]


def _allgather(blks, *, name, in_vmem):
    n = len(blks)
    space = pltpu.VMEM if in_vmem else pl.ANY

    def body(*refs):
        _gather_blocks(refs[:n], refs[n:2 * n], *refs[2 * n:])

    return pl.pallas_call(
        body, name=name, out_shape=[jax.ShapeDtypeStruct((8,) + b.shape, b.dtype) for b in blks],
        in_specs=[pl.BlockSpec(memory_space=space)] * n, out_specs=[pl.BlockSpec(memory_space=space)] * n,
        scratch_shapes=_gather_sems(n))(*blks)


def _cast_place(ws, c, dev):
    n = len(ws)

    def body(s_ref, *refs):
        for u in range(n):
            refs[n + u][0] = refs[u][...].astype(BF16)

    in_specs, out_specs, out_shape = [], [], []
    for w in ws:
        q, cols = w.shape[0] // 4, w.shape[1]
        in_specs.append(pl.BlockSpec((q, cols), lambda i, s: (2 * s[0] + i, 0)))
        out_specs.append(pl.BlockSpec((1, q, cols), lambda i, s: (s[1], i, 0)))
        out_shape.append(jax.ShapeDtypeStruct((8, 2 * q, cols), BF16))
    return pl.pallas_call(
        body, name="cast_place",
        grid_spec=pltpu.PrefetchScalarGridSpec(num_scalar_prefetch=1, grid=(2,), in_specs=in_specs,
                                               out_specs=out_specs),
        out_shape=out_shape,
        compiler_params=pltpu.CompilerParams(vmem_limit_bytes=48 << 20))(jnp.stack([c, dev]), *ws)


def _gather_phases(out_refs, send_sems, recv_sems):
    n = len(out_refs)
    x, y, c = _place()
    me, sibling = (x, y, c), (x, y, 1 - c)
    chips = [(1 - x, y), (x, 1 - y), (1 - x, 1 - y)]

    def copy(u, k, block, to):
        px, py, pc = block
        ref = out_refs[u].at[4 * px + 2 * py + pc]
        return pltpu.make_async_remote_copy(src_ref=ref, dst_ref=ref, send_sem=send_sems.at[u, k],
                                            recv_sem=recv_sems.at[u, k], device_id=to, device_id_type=MESH)

    def start():
        for u in range(n):
            copy(u, 0, me, sibling).start()
            for j, chip in enumerate(chips):
                copy(u, 1 + j, me, (*chip, c)).start()

    def mid():
        for j, chip in enumerate(chips):
            for u in range(n):
                copy(u, 1 + j, (*chip, c), me).wait_recv()
                copy(u, 4 + j, (*chip, c), sibling).start()

    def end():
        for u in range(n):
            copy(u, 0, sibling, me).wait_recv()
        for j, chip in enumerate(chips):
            for u in range(n):
                copy(u, 4 + j, (*chip, 1 - c), me).wait_recv()
        for u in range(n):
            copy(u, 0, me, sibling).wait_send()
            for j, chip in enumerate(chips):
                copy(u, 1 + j, me, (*chip, c)).wait_send()
                copy(u, 4 + j, (*chip, c), sibling).wait_send()

    return start, mid, end


def _carry_gather(bufs):
    n = len(bufs)
    return _Carry(bufs, [jax.ShapeDtypeStruct(b.shape, b.dtype) for b in bufs], {u: u for u in range(n)},
                  [pltpu.SemaphoreType.DMA((n, 7)), pltpu.SemaphoreType.DMA((n, 7))],
                  lambda ins, outs, sems: _gather_phases(outs, *sems))


def _allgather_inplace(bufs, *, name):
    n = len(bufs)

    def body(*refs):
        for phase in _gather_phases(refs[n:2 * n], *refs[2 * n:]):
            phase()

    return pl.pallas_call(
        body, name=name, out_shape=[jax.ShapeDtypeStruct(b.shape, b.dtype) for b in bufs],
        in_specs=[ANY] * n, out_specs=[ANY] * n, input_output_aliases={u: u for u in range(n)},
        scratch_shapes=[pltpu.SemaphoreType.DMA((n, 7)), pltpu.SemaphoreType.DMA((n, 7))])(*bufs)


def _ag_small(raw):
    def body(dss1, dg1, dss2, dg2, dnw1, dnw2, dhw4, dqnw8, dknw2, dsk, dlg0, dlg1,
             out_ref, tot_ref, blk, send_sems, recv_sems, local_sems):
        blk[...] = jnp.zeros_like(blk)
        blk[0:2, :] = dss1[1]
        blk[2:3, :] = dg1[...]
        blk[3:5, :] = dss2[...]
        blk[5:6, :] = dg2[...]
        blk[6:8, :] = dss1[0]
        blk[8:9, :] = dnw1[...]
        blk[9:10, :] = dnw2[...]
        blk[10:11, 0:HGW] = dhw4[...]
        blk[10:11, HGW:D] = dqnw8[...]
        blk[11:12, 0:128] = dknw2[...]
        blk[12:14, 0:HGW] = dlg0[0]
        blk[14:16, 0:HGW] = dlg1[0]
        blk[16:24, 0:128] = dsk[...]
        _gather_blocks([blk], [out_ref], send_sems, recv_sems, local_sems)
        acc = out_ref[0]
        for i in range(1, 8):
            acc = acc + out_ref[i]
        tot_ref[...] = acc

    vm = pl.BlockSpec(memory_space=pltpu.VMEM)
    return pl.pallas_call(
        body, name="ag_small",
        out_shape=[jax.ShapeDtypeStruct((8, 24, D), F32), jax.ShapeDtypeStruct((24, D), F32)],
        in_specs=[vm] * 12, out_specs=[vm, vm],
        scratch_shapes=[pltpu.VMEM((24, D), F32)] + _gather_sems(1))(*raw)


def _rs_pair_exchange(units):
    n = len(units)

    def body(*refs):
        start, _, end = _pairx_phases(refs[:n], refs[n:2 * n], *refs[2 * n:])
        start()
        end()

    return pl.pallas_call(
        body, name="rs_pair_exchange", out_shape=_pairx_shapes(units),
        in_specs=[ANY] * n, out_specs=[ANY] * n,
        scratch_shapes=[pltpu.SemaphoreType.DMA((n, 4)), pltpu.SemaphoreType.DMA((n, 4))])(*units)


def _pairx_shapes(units):
    return [jax.ShapeDtypeStruct((4,) + g.shape[2:], g.dtype) for g in units]


def _pairx_phases(g_refs, r_refs, send_sems, recv_sems):
    n = len(g_refs)
    x, y, c = _place()
    cps = [pltpu.make_async_remote_copy(
        src_ref=g_refs[u].at[j, 1 - c], dst_ref=r_refs[u].at[j], send_sem=send_sems.at[u, j],
        recv_sem=recv_sems.at[u, j], device_id=(x, y, 1 - c), device_id_type=MESH)
        for u in range(n) for j in range(4)]

    def start():
        for cp in cps:
            cp.start()

    def end():
        for cp in cps:
            cp.wait()

    return start, None, end


def _carry_pairx(units):
    n = len(units)
    return _Carry(units, _pairx_shapes(units), {},
                  [pltpu.SemaphoreType.DMA((n, 4)), pltpu.SemaphoreType.DMA((n, 4))],
                  lambda ins, outs, sems: _pairx_phases(ins, outs, *sems))


def _rs_pair_add(units, recvs, c):
    n = len(units)

    def body(c_ref, *refs):
        for u in range(n):
            refs[2 * n + u][...] = (refs[u][0] + refs[n + u][...]).astype(BF16)

    in_specs, out_specs, out_shape = [], [], []
    for g in units:
        h, w = g.shape[2] // 2, g.shape[3]
        in_specs.append(pl.BlockSpec((1, 1, h, w), lambda j, i, cr: (j, cr[0], i, 0)))
    for g in units:
        h, w = g.shape[2] // 2, g.shape[3]
        in_specs.append(pl.BlockSpec((1, h, w), lambda j, i, cr: (j, i, 0)))
        out_specs.append(pl.BlockSpec((1, h, w), lambda j, i, cr: (j, i, 0)))
        out_shape.append(jax.ShapeDtypeStruct((4, 2 * h, w), BF16))
    return pl.pallas_call(
        body, name="rs_pair_add",
        grid_spec=pltpu.PrefetchScalarGridSpec(num_scalar_prefetch=1, grid=(4, 2), in_specs=in_specs,
                                               out_specs=out_specs),
        out_shape=out_shape,
        compiler_params=pltpu.CompilerParams(vmem_limit_bytes=48 << 20))(c.reshape(1), *units, *recvs)


def _rs_chip_exchange(pairs):
    n = len(pairs)

    def body(*refs):
        start, _, end = _chipx_phases(refs[:n], refs[n:2 * n], *refs[2 * n:])
        start()
        end()

    return pl.pallas_call(
        body, name="rs_chip_exchange", out_shape=[jax.ShapeDtypeStruct(p.shape, p.dtype) for p in pairs],
        in_specs=[ANY] * n, out_specs=[ANY] * n,
        scratch_shapes=[pltpu.SemaphoreType.DMA((n, 3)), pltpu.SemaphoreType.DMA((n, 3))])(*pairs)


def _chipx_phases(p_refs, r_refs, send_sems, recv_sems):
    n = len(p_refs)
    x, y, c = _place()
    k = 2 * x + y
    sends = []
    for d in range(1, 4):
        j = (k + d) % 4
        for u in range(n):
            sends.append(pltpu.make_async_remote_copy(
                src_ref=p_refs[u].at[j], dst_ref=r_refs[u].at[k], send_sem=send_sems.at[u, d - 1],
                recv_sem=recv_sems.at[u, d - 1], device_id=(j // 2, j % 2, c), device_id_type=MESH))

    def start():
        for cp in sends:
            cp.start()

    def end():
        for d in range(1, 4):
            src = (k + 4 - d) % 4
            for u in range(n):
                pltpu.make_async_remote_copy(
                    src_ref=p_refs[u].at[src], dst_ref=r_refs[u].at[src], send_sem=send_sems.at[u, d - 1],
                    recv_sem=recv_sems.at[u, d - 1], device_id=(x, y, c), device_id_type=MESH).wait_recv()
        for cp in sends:
            cp.wait_send()

    return start, None, end


def _carry_chipx(pairs):
    n = len(pairs)
    return _Carry(pairs, [jax.ShapeDtypeStruct(p.shape, p.dtype) for p in pairs], {},
                  [pltpu.SemaphoreType.DMA((n, 3)), pltpu.SemaphoreType.DMA((n, 3))],
                  lambda ins, outs, sems: _chipx_phases(ins, outs, *sems))


def _rs_chip_add(pairs, contribs, c, chip):
    n = len(pairs)

    def body(s_ref, *refs):
        for u in range(n):
            a, b, c_, d = refs[4 * u:4 * u + 4]
            refs[4 * n + u][0] = ((a[0].astype(F32) + b[0].astype(F32)) + c_[0].astype(F32)) + d[0].astype(F32)

    in_specs, out_specs, out_shape, args = [], [], [], []
    for p, r in zip(pairs, contribs):
        h, w = p.shape[1] // 2, p.shape[2]
        in_specs += [pl.BlockSpec((1, h, w), functools.partial(lambda d, i, s: ((s[1] + d) % 4, i, 0), d))
                     for d in range(4)]
        args += [p, r, r, r]
        out_specs.append(pl.BlockSpec((1, h, w), lambda i, s: (s[0], i, 0)))
        out_shape.append(jax.ShapeDtypeStruct((2, 2 * h, w), F32))
    return pl.pallas_call(
        body, name="rs_chip_add",
        grid_spec=pltpu.PrefetchScalarGridSpec(num_scalar_prefetch=1, grid=(2,), in_specs=in_specs,
                                               out_specs=out_specs),
        out_shape=out_shape,
        compiler_params=pltpu.CompilerParams(vmem_limit_bytes=48 << 20))(jnp.stack([c, chip]), *args)


def _rs_sibling_gather(reds):
    n = len(reds)

    def body(*refs):
        start, _, end = _sibx_phases(refs[n:2 * n], *refs[2 * n:])
        start()
        end()

    return pl.pallas_call(
        body, name="rs_sibling_gather", out_shape=[jax.ShapeDtypeStruct(r.shape, r.dtype) for r in reds],
        in_specs=[ANY] * n, out_specs=[ANY] * n, input_output_aliases={u: u for u in range(n)},
        scratch_shapes=[pltpu.SemaphoreType.DMA((n,))] * 2)(*reds)


def _sibx_phases(o_refs, send_sems, recv_sems):
    n = len(o_refs)
    x, y, c = _place()
    cps = [pltpu.make_async_remote_copy(
        src_ref=o_refs[u].at[c], dst_ref=o_refs[u].at[c], send_sem=send_sems.at[u], recv_sem=recv_sems.at[u],
        device_id=(x, y, 1 - c), device_id_type=MESH) for u in range(n)]

    def start():
        for cp in cps:
            cp.start()

    def end():
        for u in range(n):
            cps[u].wait_send()
            pltpu.make_async_remote_copy(
                src_ref=o_refs[u].at[1 - c], dst_ref=o_refs[u].at[1 - c], send_sem=send_sems.at[u],
                recv_sem=recv_sems.at[u], device_id=(x, y, 1 - c), device_id_type=MESH).wait_recv()

    return start, None, end


def _carry_sibx(reds):
    n = len(reds)
    return _Carry(reds, [jax.ShapeDtypeStruct(r.shape, r.dtype) for r in reds], {u: u for u in range(n)},
                  [pltpu.SemaphoreType.DMA((n,))] * 2, lambda ins, outs, sems: _sibx_phases(outs, *sems))


def _ada_fwd(c16, w, b):
    n = w.shape[1]
    tn = 512

    def body(c_ref, w_ref, b_ref, o_ref):
        cc = c_ref[...]
        o_ref[...] = _dot(cc * _sig(cc), w_ref[...], prec=HI) + b_ref[...]

    return _pcall(body, name="ada_fwd", grid=(n // tn,),
                  in_specs=[_full((16, D)), pl.BlockSpec((D, tn), lambda j: (0, j)),
                            pl.BlockSpec((1, tn), lambda j: (0, j))],
                  out_specs=pl.BlockSpec((16, tn), lambda j: (0, j)),
                  out_shape=jax.ShapeDtypeStruct((16, n), F32))(c16, w, b)


def _ada_bwd(c16, dmod16, w):
    n = w.shape[1]
    tn = 512

    def body(c_ref, d_ref, w_ref, gw_ref, gc_ref):
        j = pl.program_id(0)

        @pl.when(j == 0)
        def _():
            gc_ref[...] = jnp.zeros_like(gc_ref)

        cc = c_ref[...]
        dm = d_ref[...]
        gw_ref[...] = _dot(cc * _sig(cc), dm, TN, prec=HI)
        gc_ref[...] += _dot(dm, w_ref[...], NT, prec=HI)

    return _pcall(body, name="ada_bwd", grid=(n // tn,),
                  in_specs=[_full((16, D)), pl.BlockSpec((16, tn), lambda j: (0, j)),
                            pl.BlockSpec((D, tn), lambda j: (0, j))],
                  out_specs=[pl.BlockSpec((D, tn), lambda j: (0, j)), _full((16, D))],
                  out_shape=[jax.ShapeDtypeStruct((D, n), F32),
                             jax.ShapeDtypeStruct((16, D), F32)])(c16, dmod16, w)


def _adam_math(w, g, m, v):
    c1 = 1.0 - ADAM_B1 ** ADAM_STEP
    c2 = 1.0 - ADAM_B2 ** ADAM_STEP
    nm = ADAM_B1 * m + (1.0 - ADAM_B1) * g
    nv = ADAM_B2 * v + (1.0 - ADAM_B2) * (g * g)
    return -ADAM_LR * ((nm / c1) / (jnp.sqrt(nv / c2) + ADAM_EPS) + ADAM_WD * w), nm, nv


def _adamw_small(ws, gs, ms, vs):
    n = len(ws)

    def body(*refs):
        for u in range(n):
            d_, nm, nv = _adam_math(refs[u][...], refs[n + u][...], refs[2 * n + u][...], refs[3 * n + u][...])
            refs[4 * n + u][...] = d_
            refs[5 * n + u][...] = nm
            refs[6 * n + u][...] = nv

    specs = [_full(w.shape) for w in ws]
    shapes = [jax.ShapeDtypeStruct(w.shape, F32) for w in ws]
    out = _pcall(body, name="adamw_small", grid=(1,), in_specs=specs * 4, out_specs=specs * 3,
                 out_shape=shapes * 3)(*ws, *gs, *ms, *vs)
    return out[:n], out[n:2 * n], out[2 * n:]


def _cctx_grad(parts, c_ctx):
    def body(p_ref, c_ref, o_ref):
        acc = p_ref[0:1, :]
        for k in range(1, 4):
            acc = acc + p_ref[k:k + 1, :]
        cc = c_ref[...]
        s = _sig(cc)
        o_ref[...] = acc * (s * (1.0 + cc * (1.0 - s)))

    return _pcall(body, name="cctx_grad", grid=(1,), in_specs=[_full(parts.shape), _full((1, D))],
                  out_specs=_full((1, D)), out_shape=jax.ShapeDtypeStruct((1, D), F32))(parts, c_ctx)


ADAM_STEPS = 8


def _adamw_multi(ws, gs, ms, vs, *, name, carry=None):
    n = len(ws)

    def body(*refs):
        for u in range(n):
            refs[4 * n + u][...], refs[5 * n + u][...], refs[6 * n + u][...] = _adam_math(
                refs[u][...], refs[n + u][...], refs[2 * n + u][...], refs[3 * n + u][...])

    specs = [pl.BlockSpec((w.shape[0] // ADAM_STEPS, w.shape[1]), lambda i: (i, 0)) for w in ws]
    shapes = [jax.ShapeDtypeStruct(w.shape, F32) for w in ws]
    res = _pcall(body, name=name, grid=(ADAM_STEPS,), in_specs=specs * 4, out_specs=specs * 3,
                 out_shape=shapes * 3, carry=carry)(*ws, *gs, *ms, *vs)
    out, extra = res if carry is not None else (res, None)
    return (out[:n], out[n:2 * n], out[2 * n:]), extra


def kernel(x, c, ctx, c_ctx, w_ada, b_ada, norm_mix_w, norm_ffn_w, w_in, hgrn_lb_logits, hgrn_norm_w, q_norm_w, k_norm_w, attn_sinks, w_branch_hgrn, w_branch_attn, w_out, w_ffn_gate, w_ffn_up, w_ffn_down, loss_target, m_c_ctx, m_w_ada, m_b_ada, m_norm_mix_w, m_norm_ffn_w, m_w_in, m_hgrn_lb_logits, m_hgrn_norm_w, m_q_norm_w, m_k_norm_w, m_attn_sinks, m_w_branch_hgrn, m_w_branch_attn, m_w_out, m_w_ffn_gate, m_w_ffn_up, m_w_ffn_down, v_c_ctx, v_w_ada, v_b_ada, v_norm_mix_w, v_norm_ffn_w, v_w_in, v_hgrn_lb_logits, v_hgrn_norm_w, v_q_norm_w, v_k_norm_w, v_attn_sinks, v_w_branch_hgrn, v_w_branch_attn, v_w_out, v_w_ffn_gate, v_w_ffn_up, v_w_ffn_down):
    xi, yi, ci = _place()
    chip = 2 * xi + yi
    dev = 2 * chip + ci
    s_len = x.shape[1]

    lbrow = jnp.pad(hgrn_lb_logits.reshape(1, 512), ((0, 0), (0, D - 512)))
    blk = jnp.concatenate([c, lbrow, jnp.zeros((6, D), F32)], axis=0)
    g0, = _allgather([blk], name="ag_cond", in_vmem=True)
    c16 = jnp.concatenate([g0[:, 0], c_ctx[None], jnp.zeros((7, D), F32)], axis=0)
    lg = g0[0::2, 1, :512].reshape(4, 2, 2, 128).transpose(1, 2, 0, 3).reshape(2, 2, HGW)

    nada = w_ada.shape[2]
    b_sh = lax.dynamic_slice(b_ada, (0, chip * nada), (1, nada))
    mod_sh = _ada_fwd(c16, w_ada[0], b_sh)
    g1, = _allgather([mod_sh], name="ag_mod", in_vmem=True)
    modall = g1[0::2].transpose(1, 0, 2).reshape(16, 4 * nada)
    mod = lax.dynamic_slice(modall, (dev, 0), (1, 6 * D)).reshape(6, D)
    modc = modall[8].reshape(6, D)[:2]

    shards = [w_in[0].T, w_branch_hgrn[0], w_branch_attn[0], w_out[0], w_ffn_gate[0].T, w_ffn_up[0].T,
              w_ffn_down[0]]
    bufs = _cast_place(shards, ci, dev)
    in8, = _allgather_inplace(bufs[0:1], name="ag_w_in")

    sq, gx, _, small, rs = _local_step(
        x[0], ctx[0], loss_target[0], mod, modc, norm_mix_w, norm_ffn_w, lg, hgrn_norm_w, q_norm_w,
        k_norm_w, attn_sinks[0], in8.reshape(NCOL, D), bufs[1:], dist=(ci, chip))
    loss = lax.psum(0.5 * jnp.sum(sq) / D, ("x", "y", "c"))

    def whole(r):
        return r.reshape(2 * r.shape[1], r.shape[2])

    g_dn, g_g, g_u = [whole(r) for r in rs["ffn_done"]]
    g_bh, g_ba, g_o = [whole(r) for r in rs["mix_done"]]
    in_pairs = _rs_pair_add(rs["in_units"], rs["in_recv"], ci)

    g2, tot = _ag_small(small["raw"])
    dmodc_tot = jnp.pad(tot[6:8].reshape(1, 2 * D), ((0, 0), (0, 4 * D)))
    g_b_ada = tot[0:6].reshape(1, 6 * D) + dmodc_tot
    dmod16 = jnp.concatenate([g2[:, 0:6].reshape(8, 6 * D), dmodc_tot, jnp.zeros((7, 6 * D), F32)], axis=0)
    g_w_ada, gc_part = _ada_bwd(c16, lax.dynamic_slice(dmod16, (0, chip * nada), (16, nada)), w_ada[0])
    g3, = _allgather([gc_part[8:16]], name="ag_cctx", in_vmem=True)
    g_c_ctx = _cctx_grad(g3[0::2, 0], c_ctx[None])[0]
    g_nw1 = tot[8:9]
    g_nw2 = tot[9:10]
    g_hw = tot[10, :HGW].reshape(4, HGD).sum(0, keepdims=True)
    g_qnw = tot[10, HGW:].reshape(8, HDIM).sum(0, keepdims=True)
    g_knw = tot[11, :128].reshape(2, HDIM).sum(0, keepdims=True)
    g_sinks = tot[16:24, 0][None]
    g_lg = lax.dynamic_slice(tot[12:16, :HGW].reshape(2, 2, HGW), (0, 0, chip * 128), (2, 2, 128))

    names = ["c_ctx", "w_ada", "b_ada", "norm_mix_w", "norm_ffn_w", "w_in", "hgrn_lb_logits", "hgrn_norm_w",
             "q_norm_w", "k_norm_w", "attn_sinks", "w_branch_hgrn", "w_branch_attn", "w_out", "w_ffn_gate",
             "w_ffn_up", "w_ffn_down"]
    ws = dict(zip(names, [c_ctx, w_ada, b_ada, norm_mix_w, norm_ffn_w, w_in, hgrn_lb_logits, hgrn_norm_w,
                          q_norm_w, k_norm_w, attn_sinks, w_branch_hgrn, w_branch_attn, w_out, w_ffn_gate,
                          w_ffn_up, w_ffn_down]))
    ms = dict(zip(names, [m_c_ctx, m_w_ada, m_b_ada, m_norm_mix_w, m_norm_ffn_w, m_w_in, m_hgrn_lb_logits,
                          m_hgrn_norm_w, m_q_norm_w, m_k_norm_w, m_attn_sinks, m_w_branch_hgrn,
                          m_w_branch_attn, m_w_out, m_w_ffn_gate, m_w_ffn_up, m_w_ffn_down]))
    vs = dict(zip(names, [v_c_ctx, v_w_ada, v_b_ada, v_norm_mix_w, v_norm_ffn_w, v_w_in, v_hgrn_lb_logits,
                          v_hgrn_norm_w, v_q_norm_w, v_k_norm_w, v_attn_sinks, v_w_branch_hgrn,
                          v_w_branch_attn, v_w_out, v_w_ffn_gate, v_w_ffn_up, v_w_ffn_down]))
    transposed = ("w_in", "w_ffn_gate", "w_ffn_up")

    def view(a, n):
        return a[0].T if n in transposed else a[0]

    def unview(a, n):
        return a.T[None] if n in transposed else a[None]

    delta, new_m, new_v, grads = {}, {}, {}, {}

    def big_adamw(group, gs, name, carry=None):
        (d_, m_, v_), extra = _adamw_multi([view(ws[n], n) for n in group], gs, [view(ms[n], n) for n in group],
                                           [view(vs[n], n) for n in group], name=name, carry=carry)
        for i, n in enumerate(group):
            grads[n], delta[n], new_m[n], new_v[n] = (unview(gs[i], n), unview(d_[i], n), unview(m_[i], n),
                                                      unview(v_[i], n))
        return extra

    in_contribs = big_adamw(["w_ffn_down", "w_ffn_gate", "w_ffn_up", "w_out", "w_branch_hgrn", "w_branch_attn"],
                            [g_dn, g_g, g_u, g_o, g_bh, g_ba], "adamw_first", carry=_carry_chipx(in_pairs))
    in_reds = _rs_chip_add(in_pairs, in_contribs, ci, chip)
    g_in, = [whole(r) for r in _rs_sibling_gather(in_reds)]
    big_adamw(["w_in", "w_ada"], [g_in, g_w_ada], "adamw_second")
    grads.update(c_ctx=g_c_ctx, b_ada=g_b_ada, norm_mix_w=g_nw1, norm_ffn_w=g_nw2, hgrn_lb_logits=g_lg,
                 hgrn_norm_w=g_hw, q_norm_w=g_qnw, k_norm_w=g_knw, attn_sinks=g_sinks)
    small_names = [n for n in names if n not in delta]

    def two_d(a):
        return a.reshape(1, -1) if a.ndim == 1 else a

    sd, sm_, sv = _adamw_small(*[[two_d(d[n]) for n in small_names] for d in (ws, grads, ms, vs)])
    for i, n in enumerate(small_names):
        for dst, src in ((delta, sd), (new_m, sm_), (new_v, sv)):
            dst[n] = src[i].reshape(ws[n].shape)
    return (loss, gx[None], *[grads[n] for n in names], *[delta[n] for n in names],
            *[new_m[n] for n in names], *[new_v[n] for n in names])
```

```python
import functools

import numpy as np
import jax
import jax.numpy as jnp
from jax import lax
from jax.experimental import pallas as pl
from jax.experimental.pallas import tpu as pltpu

F32 = jnp.float32
BF16 = jnp.bfloat16
HI = lax.Precision.HIGHEST
MESH = pl.DeviceIdType.MESH

D = 1024
L = 256
TM = 256
HGW = 512
HGD = 128
CH = 32
ATW = 512
HDIM = 64
BLK = 128
GRID_W = 64
DFF = 2816
NCOL = 5376
EPS = 1e-6
ROPE_THETA = 10000.0

C_FB, C_INP, C_QHG, C_FF = 0, 1, 2, 3
C_GATES = 1
C_GHG, C_QRAW = 8, 9
C_KV = 20
C_QKV = 6

ADAM_LR, ADAM_B1, ADAM_B2, ADAM_EPS, ADAM_WD, ADAM_STEP = 0.001, 0.9, 0.999, 1e-08, 0.01, 10

NN = (((1,), (0,)), ((), ()))
NT = (((1,), (1,)), ((), ()))
TN = (((0,), (0,)), ((), ()))


def _dot(a, b, dims=NN, prec=None):
    return lax.dot_general(a, b, dims, precision=prec, preferred_element_type=F32)


def _bdot(a, b, dims=NN):
    return _dot(a.astype(BF16), b.astype(BF16), dims)


def _sig(x):
    return 1.0 / (1.0 + jnp.exp(-x))


class _Carry:
    def __init__(self, ins, outs, aliases, scratch, phases):
        self.ins, self.outs, self.aliases, self.scratch, self.phases = ins, outs, aliases, scratch, phases


def _in_hbm(args):
    return [pltpu.with_memory_space_constraint(a, pltpu.HBM) for a in args]


def _carry_join(a, b):
    na_in, na_out, na_sc = len(a.ins), len(a.outs), len(a.scratch)
    aliases = dict(a.aliases)
    aliases.update({na_in + i: na_out + o for i, o in b.aliases.items()})

    def phases(ins, outs, sems):
        pa = a.phases(ins[:na_in], outs[:na_out], sems[:na_sc])
        pb = b.phases(ins[na_in:], outs[na_out:], sems[na_sc:])

        def both(fa, fb):
            if fa is None and fb is None:
                return None

            def run():
                for fn in (fa, fb):
                    if fn is not None:
                        fn()
            return run

        return tuple(both(fa, fb) for fa, fb in zip(pa, pb))

    return _Carry(list(a.ins) + list(b.ins), list(a.outs) + list(b.outs), aliases,
                  list(a.scratch) + list(b.scratch), phases)


def _pcall(body, *, name, grid, in_specs, out_specs, out_shape, scratch=(), aliases=None, vmem_mb=48,
           carry=None):
    params = pltpu.CompilerParams(dimension_semantics=("arbitrary",) * len(grid),
                                  vmem_limit_bytes=vmem_mb << 20)
    if carry is None:
        plain = pl.pallas_call(
            body, name=name, grid=grid, in_specs=in_specs, out_specs=out_specs, out_shape=out_shape,
            scratch_shapes=list(scratch), input_output_aliases=aliases or {}, compiler_params=params)
        return lambda *args: plain(*_in_hbm(args))
    single = not isinstance(out_shape, (list, tuple))
    out_specs_l = [out_specs] if single else list(out_specs)
    out_shape_l = [out_shape] if single else list(out_shape)
    n_in, n_out, n_sc = len(in_specs), len(out_shape_l), len(scratch)
    k_in, k_out = len(carry.ins), len(carry.outs)
    nsteps = int(np.prod(grid))
    assert nsteps >= 3

    def wrapped(*refs):
        ins, cins = refs[:n_in], refs[n_in:n_in + k_in]
        o0 = n_in + k_in
        outs, couts = refs[o0:o0 + n_out], refs[o0 + n_out:o0 + n_out + k_out]
        s0 = o0 + n_out + k_out
        sc, csc = refs[s0:s0 + n_sc], refs[s0 + n_sc:]
        step = pl.program_id(0)
        for ax in range(1, len(grid)):
            step = step * grid[ax] + pl.program_id(ax)
        start, mid, end = carry.phases(cins, couts, csc)
        pl.when(step == 0)(start)
        body(*ins, *outs, *sc)
        if mid is not None:
            pl.when(step == nsteps - 2)(mid)
        pl.when(step == nsteps - 1)(end)

    all_aliases = dict(aliases or {})
    all_aliases.update({n_in + i: n_out + o for i, o in carry.aliases.items()})
    call = pl.pallas_call(
        wrapped, name=name, grid=grid, in_specs=list(in_specs) + [ANY] * k_in,
        out_specs=out_specs_l + [ANY] * k_out, out_shape=out_shape_l + list(carry.outs),
        scratch_shapes=list(scratch) + list(carry.scratch), input_output_aliases=all_aliases,
        compiler_params=params)

    def run(*args):
        res = call(*_in_hbm(args), *carry.ins)
        core = res[:n_out]
        return (core[0] if single else list(core)), list(res[n_out:])

    return run


def _full(shape):
    nd = len(shape)
    return pl.BlockSpec(shape, lambda *_: (0,) * nd)


ANY = pl.BlockSpec(memory_space=pl.ANY)


def _mm(a, b, *, name, mode="nn", out_dtype=F32, tm, tn, tk):
    if mode == "nn":
        (m, k), (k2, n) = a.shape, b.shape
    elif mode == "nt":
        (m, k), (n, k2) = a.shape, b.shape
    else:
        (k, m), (k2, n) = a.shape, b.shape
    assert k == k2 and m % tm == 0 and n % tn == 0 and k % tk == 0, (name, a.shape, b.shape)
    nk = k // tk
    dims = {"nn": NN, "nt": NT, "tn": TN}[mode]

    def body(a_ref, b_ref, o_ref, acc):
        kk = pl.program_id(2)

        @pl.when(kk == 0)
        def _():
            acc[...] = jnp.zeros_like(acc)

        acc[...] += _bdot(a_ref[...], b_ref[...], dims)

        @pl.when(kk == nk - 1)
        def _():
            o_ref[...] = acc[...].astype(out_dtype)

    a_spec = (pl.BlockSpec((tk, tm), lambda i, j, kk: (kk, i)) if mode == "tn"
              else pl.BlockSpec((tm, tk), lambda i, j, kk: (i, kk)))
    b_spec = (pl.BlockSpec((tn, tk), lambda i, j, kk: (j, kk)) if mode == "nt"
              else pl.BlockSpec((tk, tn), lambda i, j, kk: (kk, j)))
    return _pcall(body, name=name, grid=(m // tm, n // tn, nk), in_specs=[a_spec, b_spec],
                  out_specs=pl.BlockSpec((tm, tn), lambda i, j, kk: (i, j)),
                  out_shape=jax.ShapeDtypeStruct((m, n), out_dtype),
                  scratch=[pltpu.VMEM((tm, tn), F32)])(a, b)


NT_IN = NCOL // 256


def _src_block(j):
    return j + jnp.where(j < 4, 2, jnp.where(j < 6, 3, jnp.where(j < 8, -6, jnp.where(
        j < 16, 5, jnp.where(j < 20, -7, -14)))))


def _mm_in(h, wt, tm, carry=None):
    tt = h.shape[0]

    def body(h_ref, w_ref, o_ref):
        o_ref[...] = _bdot(h_ref[...], w_ref[...], NT)

    return _pcall(body, name="mm_in", grid=(tt // tm, NT_IN),
                  in_specs=[pl.BlockSpec((tm, D), lambda i, j: (i, 0)),
                            pl.BlockSpec((256, D), lambda i, j: (_src_block(j), 0))],
                  out_specs=pl.BlockSpec((tm, 256), lambda i, j: (i, j)),
                  out_shape=jax.ShapeDtypeStruct((tt, NCOL), F32), carry=carry)(h, wt)


def _mm_dh(dp, wt, tm):
    tt = dp.shape[0]

    def body(d_ref, w_ref, o_ref, acc):
        kk = pl.program_id(1)

        @pl.when(kk == 0)
        def _():
            acc[...] = jnp.zeros_like(acc)

        acc[...] += _bdot(d_ref[...], w_ref[...])

        @pl.when(kk == NT_IN - 1)
        def _():
            o_ref[...] = acc[...]

    return _pcall(body, name="mm_dh", grid=(tt // tm, NT_IN),
                  in_specs=[pl.BlockSpec((tm, 256), lambda i, kk: (i, kk)),
                            pl.BlockSpec((256, D), lambda i, kk: (_src_block(kk), 0))],
                  out_specs=pl.BlockSpec((tm, D), lambda i, kk: (i, 0)),
                  out_shape=jax.ShapeDtypeStruct((tt, D), F32), scratch=[pltpu.VMEM((tm, D), F32)])(dp, wt)


def _mm_gin(dp, h, tk):
    tt = dp.shape[0]
    nk = tt // tk

    def body(d_ref, h_ref, o_ref, acc):
        kk = pl.program_id(1)

        @pl.when(kk == 0)
        def _():
            acc[...] = jnp.zeros_like(acc)

        acc[...] += _bdot(d_ref[...], h_ref[...], TN)

        @pl.when(kk == nk - 1)
        def _():
            o_ref[...] = acc[...]

    return _pcall(body, name="mm_gin", grid=(NT_IN, nk),
                  in_specs=[pl.BlockSpec((tk, 256), lambda j, kk: (kk, j)),
                            pl.BlockSpec((tk, D), lambda j, kk: (kk, 0))],
                  out_specs=pl.BlockSpec((256, D), lambda j, kk: (_src_block(j), 0)),
                  out_shape=jax.ShapeDtypeStruct((NCOL, D), F32), scratch=[pltpu.VMEM((256, D), F32)])(dp, h)


def _modulate(xin, nw, ss, *, name, sel):
    rows = xin.shape[0]

    def body(x_ref, nw_ref, ss_ref, h_ref):
        x = x_ref[...]
        r = lax.rsqrt(jnp.mean(x * x, axis=-1, keepdims=True) + EPS)
        s = ss_ref[0]
        h_ref[...] = ((x * r * nw_ref[...]) * (1.0 + s[1:2]) + s[0:1]).astype(BF16)

    return _pcall(body, name=name, grid=(rows // TM,),
                  in_specs=[pl.BlockSpec((TM, D), lambda i: (i, 0)), _full((1, D)),
                            pl.BlockSpec((1, 2, D), lambda i: (sel(i), 0, 0))],
                  out_specs=pl.BlockSpec((TM, D), lambda i: (i, 0)),
                  out_shape=jax.ShapeDtypeStruct((rows, D), BF16))(xin, nw, ss)


def _norm_bwd_rows(x, dh, nw, scale):
    r = lax.rsqrt(jnp.mean(x * x, axis=-1, keepdims=True) + EPS)
    xh = x * r
    dxh = dh * ((1.0 + scale) * nw)
    dx = r * (dxh - xh * jnp.mean(dxh * xh, axis=-1, keepdims=True))
    return dx, xh


def _res1_mod2(x, ao, g1, nw2, ss2):
    s_len = x.shape[0]

    def body(x_ref, ao_ref, g_ref, nw_ref, ss_ref, x1_ref, h_ref):
        x1 = x_ref[...] + g_ref[...] * ao_ref[...]
        x1_ref[...] = x1
        r = lax.rsqrt(jnp.mean(x1 * x1, axis=-1, keepdims=True) + EPS)
        s = ss_ref[0]
        h_ref[...] = ((x1 * r * nw_ref[...]) * (1.0 + s[1:2]) + s[0:1]).astype(BF16)

    row = pl.BlockSpec((TM, D), lambda i: (i, 0))
    return _pcall(body, name="res1_mod2", grid=(s_len // TM,),
                  in_specs=[row, row, _full((1, D)), _full((1, D)), _full((1, 2, D))],
                  out_specs=[row, row],
                  out_shape=[jax.ShapeDtypeStruct((s_len, D), F32),
                             jax.ShapeDtypeStruct((s_len, D), BF16)])(x, ao, g1, nw2, ss2)


TS = 1024


def _acc_call(body, *, name, grid, in_specs, out_specs, out_shape, acc_shapes, args):
    return _pcall(body, name=name, grid=grid, in_specs=in_specs, out_specs=out_specs, out_shape=out_shape,
                  scratch=[pltpu.VMEM(s, F32) for s in acc_shapes])(*args)


def _mm_cs(a, w4, *, name):
    m, k = a.shape
    _, _, ns = w4.shape

    def body(a_ref, w_ref, o_ref):
        o_ref[...] = _bdot(a_ref[...], w_ref[0])

    return _pcall(body, name=name, grid=(m // TS, 4),
                  in_specs=[pl.BlockSpec((TS, k), lambda i, j: (i, 0)),
                            pl.BlockSpec((1, k, ns), lambda i, j: (j, 0, 0))],
                  out_specs=pl.BlockSpec((TS, ns), lambda i, j: (i, j)),
                  out_shape=jax.ShapeDtypeStruct((m, 4 * ns), F32))(a, w4)


def _mm_cs_nt(a, w4, *, name):
    m = a.shape[0]
    _, k, ns = w4.shape

    def body(a_ref, w_ref, o_ref, acc):
        j = pl.program_id(1)

        @pl.when(j == 0)
        def _():
            acc[...] = jnp.zeros_like(acc)

        acc[...] += _bdot(a_ref[...], w_ref[0], NT)

        @pl.when(j == 3)
        def _():
            o_ref[...] = acc[...]

    return _acc_call(body, name=name, grid=(m // TS, 4),
                     in_specs=[pl.BlockSpec((TS, ns), lambda i, j: (i, j)),
                               pl.BlockSpec((1, k, ns), lambda i, j: (j, 0, 0))],
                     out_specs=pl.BlockSpec((TS, k), lambda i, j: (i, 0)),
                     out_shape=jax.ShapeDtypeStruct((m, k), F32), acc_shapes=[(TS, k)], args=(a, w4))


def _mm_cs_tn(a, b, ns, *, name):
    s_len, k = a.shape
    nk = s_len // TS

    def body(a_ref, b_ref, o_ref, acc):
        t = pl.program_id(1)

        @pl.when(t == 0)
        def _():
            acc[...] = jnp.zeros_like(acc)

        acc[...] += _bdot(a_ref[...], b_ref[...], TN)

        @pl.when(t == nk - 1)
        def _():
            o_ref[0] = acc[...]

    return _acc_call(body, name=name, grid=(4, nk),
                     in_specs=[pl.BlockSpec((TS, k), lambda j, t: (t, 0)),
                               pl.BlockSpec((TS, ns), lambda j, t: (t, j))],
                     out_specs=pl.BlockSpec((1, k, ns), lambda j, t: (j, 0, 0)),
                     out_shape=jax.ShapeDtypeStruct((4, k, ns), F32), acc_shapes=[(k, ns)], args=(a, b))


def _ffn_up(h2, g4, u4):
    s_len = h2.shape[0]
    ns = g4.shape[1]

    def body(h_ref, g_ref, u_ref, a_ref, b_ref, z_ref):
        h = h_ref[...]
        a = _bdot(h, g_ref[0], NT)
        b = _bdot(h, u_ref[0], NT)
        a_ref[0] = a.astype(BF16)
        b_ref[0] = b.astype(BF16)
        z_ref[0] = (a * _sig(a) * b).astype(BF16)

    w = pl.BlockSpec((1, ns, D), lambda i, j: (j, 0, 0))
    o = pl.BlockSpec((1, TS, ns), lambda i, j: (j, i, 0))
    f = jax.ShapeDtypeStruct((4, s_len, ns), BF16)
    return _pcall(body, name="ffn_up", grid=(s_len // TS, 4),
                  in_specs=[pl.BlockSpec((TS, D), lambda i, j: (i, 0)), w, w], out_specs=[o, o, o],
                  out_shape=[f, f, jax.ShapeDtypeStruct((4, s_len, ns), BF16)])(h2, g4, u4)


def _ffn_down(z4, dn4):
    _, s_len, ns = z4.shape

    def body(z_ref, w_ref, o_ref, acc):
        j = pl.program_id(1)

        @pl.when(j == 0)
        def _():
            acc[...] = jnp.zeros_like(acc)

        acc[...] += _bdot(z_ref[0], w_ref[0])

        @pl.when(j == 3)
        def _():
            o_ref[...] = acc[...]

    return _acc_call(body, name="ffn_down", grid=(s_len // TS, 4),
                     in_specs=[pl.BlockSpec((1, TS, ns), lambda i, j: (j, i, 0)),
                               pl.BlockSpec((1, ns, D), lambda i, j: (j, 0, 0))],
                     out_specs=pl.BlockSpec((TS, D), lambda i, j: (i, 0)),
                     out_shape=jax.ShapeDtypeStruct((s_len, D), F32), acc_shapes=[(TS, D)], args=(z4, dn4))


def _ffn_dz(dyb, dn4, a4, b4):
    _, s_len, ns = a4.shape

    def body(dy_ref, w_ref, a_ref, b_ref, da_ref, db_ref):
        dz = _bdot(dy_ref[...], w_ref[0], NT)
        a = a_ref[0].astype(F32)
        s = _sig(a)
        da_ref[0] = (dz * b_ref[0].astype(F32) * (s * (1.0 + a * (1.0 - s)))).astype(BF16)
        db_ref[0] = (dz * (a * s)).astype(BF16)

    t = pl.BlockSpec((1, TS, ns), lambda i, j: (j, i, 0))
    o = jax.ShapeDtypeStruct((4, s_len, ns), BF16)
    return _pcall(body, name="ffn_dz", grid=(s_len // TS, 4),
                  in_specs=[pl.BlockSpec((TS, D), lambda i, j: (i, 0)),
                            pl.BlockSpec((1, ns, D), lambda i, j: (j, 0, 0)), t, t],
                  out_specs=[t, t], out_shape=[o, o])(dyb, dn4, a4, b4)


def _ffn_gdn(z4, dyb):
    _, s_len, ns = z4.shape
    nk = s_len // TS

    def body(z_ref, dy_ref, o_ref, acc):
        t = pl.program_id(1)

        @pl.when(t == 0)
        def _():
            acc[...] = jnp.zeros_like(acc)

        acc[...] += _bdot(z_ref[0], dy_ref[...], TN)

        @pl.when(t == nk - 1)
        def _():
            o_ref[0] = acc[...]

    return _acc_call(body, name="ffn_gdn", grid=(4, nk),
                     in_specs=[pl.BlockSpec((1, TS, ns), lambda j, t: (j, t, 0)),
                               pl.BlockSpec((TS, D), lambda j, t: (t, 0))],
                     out_specs=pl.BlockSpec((1, ns, D), lambda j, t: (j, 0, 0)),
                     out_shape=jax.ShapeDtypeStruct((4, ns, D), F32), acc_shapes=[(ns, D)], args=(z4, dyb))


def _ffn_dh2(da4, db4, g4, u4):
    _, s_len, ns = da4.shape

    def body(da_ref, db_ref, g_ref, u_ref, o_ref, acc):
        j = pl.program_id(1)

        @pl.when(j == 0)
        def _():
            acc[...] = jnp.zeros_like(acc)

        acc[...] += _bdot(da_ref[0], g_ref[0]) + _bdot(db_ref[0], u_ref[0])

        @pl.when(j == 3)
        def _():
            o_ref[...] = acc[...]

    t = pl.BlockSpec((1, TS, ns), lambda i, j: (j, i, 0))
    w = pl.BlockSpec((1, ns, D), lambda i, j: (j, 0, 0))
    return _acc_call(body, name="ffn_dh2", grid=(s_len // TS, 4), in_specs=[t, t, w, w],
                     out_specs=pl.BlockSpec((TS, D), lambda i, j: (i, 0)),
                     out_shape=jax.ShapeDtypeStruct((s_len, D), F32), acc_shapes=[(TS, D)],
                     args=(da4, db4, g4, u4))


def _ffn_ggu(h2, da4, db4):
    _, s_len, ns = da4.shape
    nk = s_len // TS

    def body(h_ref, da_ref, db_ref, gg_ref, gu_ref, acc_g, acc_u):
        t = pl.program_id(1)

        @pl.when(t == 0)
        def _():
            acc_g[...] = jnp.zeros_like(acc_g)
            acc_u[...] = jnp.zeros_like(acc_u)

        h = h_ref[...]
        acc_g[...] += _bdot(da_ref[0], h, TN)
        acc_u[...] += _bdot(db_ref[0], h, TN)

        @pl.when(t == nk - 1)
        def _():
            gg_ref[0] = acc_g[...]
            gu_ref[0] = acc_u[...]

    d = pl.BlockSpec((1, TS, ns), lambda j, t: (j, t, 0))
    o = pl.BlockSpec((1, ns, D), lambda j, t: (j, 0, 0))
    f = jax.ShapeDtypeStruct((4, ns, D), F32)
    return _acc_call(body, name="ffn_ggu", grid=(4, nk),
                     in_specs=[pl.BlockSpec((TS, D), lambda j, t: (t, 0)), d, d], out_specs=[o, o],
                     out_shape=[f, f], acc_shapes=[(ns, D), (ns, D)], args=(h2, da4, db4))


def _loss_head(x1, y, g2, tgt):
    s_len = x1.shape[0]

    def body(x1_ref, y_ref, g_ref, t_ref, sq_ref, dx2_ref, dyb_ref, dg_ref):
        i = pl.program_id(0)

        @pl.when(i == 0)
        def _():
            sq_ref[...] = jnp.zeros_like(sq_ref)
            dg_ref[...] = jnp.zeros_like(dg_ref)

        y_ = y_ref[...]
        g = g_ref[...]
        e = x1_ref[...] + g * y_ - t_ref[...]
        sq_ref[...] += jnp.sum(e * e, axis=0, keepdims=True)
        dx2 = e * (1.0 / D)
        dx2_ref[...] = dx2
        dyb_ref[...] = (g * dx2).astype(BF16)
        dg_ref[...] += jnp.sum(dx2 * y_, axis=0, keepdims=True)

    row = pl.BlockSpec((TM, D), lambda i: (i, 0))
    vec = _full((1, D))
    return _pcall(body, name="loss_head", grid=(s_len // TM,),
                  in_specs=[row, row, vec, row], out_specs=[vec, row, row, vec],
                  out_shape=[jax.ShapeDtypeStruct((1, D), F32), jax.ShapeDtypeStruct((s_len, D), F32),
                             jax.ShapeDtypeStruct((s_len, D), BF16),
                             jax.ShapeDtypeStruct((1, D), F32)])(x1, y, g2, tgt)


def _mod2_bwd(x1, dh2, dx2, ao, nw2, ss2, g1):
    s_len = x1.shape[0]

    def body(x1_ref, dh_ref, dx2_ref, ao_ref, nw_ref, ss_ref, g_ref,
             dx1_ref, da_ref, dss_ref, dnw_ref, dg_ref):
        i = pl.program_id(0)

        @pl.when(i == 0)
        def _():
            dss_ref[...] = jnp.zeros_like(dss_ref)
            dnw_ref[...] = jnp.zeros_like(dnw_ref)
            dg_ref[...] = jnp.zeros_like(dg_ref)

        dh = dh_ref[...]
        nw = nw_ref[...]
        scale = ss_ref[0][1:2]
        dxn, xh = _norm_bwd_rows(x1_ref[...], dh, nw, scale)
        dx1 = dx2_ref[...] + dxn
        dx1_ref[...] = dx1
        da_ref[...] = (g_ref[...] * dx1).astype(BF16)
        dg_ref[...] += jnp.sum(dx1 * ao_ref[...], axis=0, keepdims=True)
        dsh = jnp.sum(dh, axis=0, keepdims=True)
        dsc = jnp.sum(dh * xh * nw, axis=0, keepdims=True)
        dss_ref[...] += jnp.concatenate([dsh, dsc], axis=0)
        dnw_ref[...] += jnp.sum(dh * xh * (1.0 + scale), axis=0, keepdims=True)

    row = pl.BlockSpec((TM, D), lambda i: (i, 0))
    vec = _full((1, D))
    return _pcall(body, name="mod2_bwd", grid=(s_len // TM,),
                  in_specs=[row, row, row, row, vec, _full((1, 2, D)), vec],
                  out_specs=[row, row, _full((2, D)), vec, vec],
                  out_shape=[jax.ShapeDtypeStruct((s_len, D), F32), jax.ShapeDtypeStruct((s_len, D), BF16),
                             jax.ShapeDtypeStruct((2, D), F32), jax.ShapeDtypeStruct((1, D), F32),
                             jax.ShapeDtypeStruct((1, D), F32)])(x1, dh2, dx2, ao, nw2, ss2, g1)


def _mod1_bwd(tok, dh, dx1, nw1, ss1, carry=None):
    tt = tok.shape[0]
    s_len = dx1.shape[0]

    def body(t_ref, dh_ref, dx1_ref, nw_ref, ss_ref, dx_ref, dss_ref, dnw_ref):
        i = pl.program_id(0)

        @pl.when(i == 0)
        def _():
            dnw_ref[...] = jnp.zeros_like(dnw_ref)

        @pl.when(i <= 1)
        def _():
            dss_ref[...] = jnp.zeros_like(dss_ref)

        dh_ = dh_ref[...]
        nw = nw_ref[...]
        scale = ss_ref[0][1:2]
        dxn, xh = _norm_bwd_rows(t_ref[...], dh_, nw, scale)

        @pl.when(i >= 1)
        def _():
            dx_ref[...] = dx1_ref[...] + dxn

        dsh = jnp.sum(dh_, axis=0, keepdims=True)
        dsc = jnp.sum(dh_ * xh * nw, axis=0, keepdims=True)
        dss_ref[...] += jnp.concatenate([dsh, dsc], axis=0)[None]
        dnw_ref[...] += jnp.sum(dh_ * xh * (1.0 + scale), axis=0, keepdims=True)

    row = pl.BlockSpec((TM, D), lambda i: (i, 0))
    lat = pl.BlockSpec((TM, D), lambda i: (jnp.maximum(i - 1, 0), 0))
    sel = pl.BlockSpec((1, 2, D), lambda i: (jnp.minimum(i, 1), 0, 0))
    return _pcall(body, name="mod1_bwd", grid=(tt // TM,),
                  in_specs=[row, row, lat, _full((1, D)), sel],
                  out_specs=[lat, sel, _full((1, D))],
                  out_shape=[jax.ShapeDtypeStruct((s_len, D), F32), jax.ShapeDtypeStruct((2, 2, D), F32),
                             jax.ShapeDtypeStruct((1, D), F32)], carry=carry)(tok, dh, dx1, nw1, ss1)


def _rows(c):
    return slice(c * CH, (c + 1) * CH)


def _chunk_masks(rev, transpose=False):
    r = lax.broadcasted_iota(jnp.int32, (TM, TM), 0)
    c = lax.broadcasted_iota(jnp.int32, (TM, TM), 1)
    same = (r // CH) == (c // CH)
    before = (c >= r) if (rev != transpose) else (c <= r)
    return same & before, same


def _hgrn_gate(fl, qraw, lg):
    lb = 1.0 / (1.0 + jnp.exp(lg[1:2] - lg[0:1]))
    sg = _sig(fl)
    f = lb + (1.0 - lb) * sg
    q = qraw * _sig(qraw) * (HGD ** -0.5)
    return lb, sg, f, q


def _hgrn_fwd(p, lg, *, rev, carry=None):
    tt = p.shape[0]
    nt = tt // TM
    ncht = TM // CH
    d = 1 if rev else 0

    def tile_of(s):
        return jnp.where(s == 0, 0, nt - s) if rev else s

    def body(f_ref, inp_ref, q_ref, lg_ref, o_ref, st_ref, state):
        s = pl.program_id(0)

        @pl.when(s == 0)
        def _():
            state[...] = jnp.zeros_like(state)

        _, _, f, q = _hgrn_gate(f_ref[...], q_ref[...], lg_ref[0])
        lf = jnp.log(f)
        causal, same = _chunk_masks(rev)
        cum = _dot(causal.astype(F32), lf, prec=HI)
        tot = _dot(same.astype(F32), lf, prec=HI)
        qd = (q * jnp.exp(cum)).astype(BF16)
        kd = ((1.0 - f) * jnp.exp(-cum)).astype(BF16)
        ke = ((1.0 - f) * jnp.exp(tot - cum)).astype(BF16)
        et = jnp.exp(tot)
        v = inp_ref[...].astype(BF16)
        order = range(ncht - 1, -1, -1) if rev else range(ncht)
        outs = []
        for h in range(4):
            sl = slice(h * HGD, (h + 1) * HGD)
            qd_, kd_, ke_, v_ = qd[:, sl], kd[:, sl], ke[:, sl], v[:, sl]
            pm = jnp.where(causal, _dot(qd_, kd_, NT), 0.0).astype(BF16)
            o_h = _dot(pm, v_)
            upd = [_dot(v_[_rows(c)], ke_[_rows(c)], TN) for c in range(ncht)]
            st = state[h]
            for c in order:
                st_ref[c, h] = st
                st = st * et[c * CH:c * CH + 1, sl] + upd[c]
            state[h] = st
            inter = [_dot(qd_[_rows(c)], st_ref[c, h].astype(BF16), NT) for c in range(ncht)]
            outs.append(o_h + jnp.concatenate(inter, axis=0))
        o_ref[...] = jnp.concatenate(outs, axis=1)

    def col(cb):
        return pl.BlockSpec((TM, HGW), lambda s: (tile_of(s), cb))

    return _pcall(
        body, name="hgrn_fwd_rev" if rev else "hgrn_fwd", grid=(nt,),
        in_specs=[col(C_FB if rev else C_FF), col(C_INP), col(C_QHG),
                  pl.BlockSpec((1, 2, HGW), lambda s: (d, 0, 0))],
        out_specs=[pl.BlockSpec((TM, HGW), lambda s: (tile_of(s), 0)),
                   pl.BlockSpec((ncht, 4, HGD, HGD), lambda s: (tile_of(s), 0, 0, 0))],
        out_shape=[jax.ShapeDtypeStruct((tt, HGW), F32),
                   jax.ShapeDtypeStruct((nt * ncht, 4, HGD, HGD), F32)],
        scratch=[pltpu.VMEM((4, HGD, HGD), F32)], carry=carry)(p, p, p, lg)


def _hgrn_bwd(p, lg, do, st, dp, prev, *, rev, carry=None):
    tt = p.shape[0]
    nt = tt // TM
    ncht = TM // CH
    d = 1 if rev else 0
    second = prev is not None

    def tile_of(s):
        return jnp.where(s == nt - 1, 0, s + 1) if rev else nt - 1 - s

    def body(*refs):
        if second:
            (f_ref, inp_ref, q_ref, lg_ref, do_ref, st_ref, dvp_ref, dqp_ref, _dp_in,
             dp_ref, dlg_ref, dstate) = refs
        else:
            (f_ref, inp_ref, q_ref, lg_ref, do_ref, st_ref, _dp_in,
             dp_ref, dv_ref, dq_ref, dlg_ref, dstate) = refs
        s = pl.program_id(0)
        tile = tile_of(s)

        @pl.when(s == 0)
        def _():
            dstate[...] = jnp.zeros_like(dstate)
            dlg_ref[...] = jnp.zeros_like(dlg_ref)

        qraw = q_ref[...]
        lb, sg, f, q = _hgrn_gate(f_ref[...], qraw, lg_ref[0])
        lf = jnp.log(f)
        causal, same = _chunk_masks(rev)
        causal_t, _ = _chunk_masks(rev, transpose=True)
        cum = _dot(causal.astype(F32), lf, prec=HI)
        tot = _dot(same.astype(F32), lf, prec=HI)
        ea, eb, ee, et = jnp.exp(cum), jnp.exp(-cum), jnp.exp(tot - cum), jnp.exp(tot)
        qdf, kdf, kef = q * ea, (1.0 - f) * eb, (1.0 - f) * ee
        qd, kd, ke = qdf.astype(BF16), kdf.astype(BF16), kef.astype(BF16)
        v = inp_ref[...].astype(BF16)
        dob = jnp.where(tile == 0, 0.0, do_ref[...]).astype(BF16)
        order = range(ncht) if rev else range(ncht - 1, -1, -1)
        dq_l, dk_l, dv_l, dcum_l, dtot_l = [], [], [], [], []
        for h in range(4):
            sl = slice(h * HGD, (h + 1) * HGD)
            qd_, kd_, ke_, v_, do_ = qd[:, sl], kd[:, sl], ke[:, sl], v[:, sl], dob[:, sl]
            pmt = jnp.where(causal_t, _dot(kd_, qd_, NT), 0.0).astype(BF16)
            dpm = jnp.where(causal, _dot(do_, v_, NT), 0.0).astype(BF16)
            dpmt = jnp.where(causal_t, _dot(v_, do_, NT), 0.0).astype(BF16)
            dv = _dot(pmt, do_)
            dqd = _dot(dpm, kd_)
            dkd = _dot(dpmt, qd_)
            upd = [_dot(do_[_rows(c)], qd_[_rows(c)], TN) for c in range(ncht)]
            ds = dstate[h]
            ds1 = [None] * ncht
            for c in order:
                ds1[c] = ds
                ds = ds * et[c * CH:c * CH + 1, sl] + upd[c]
            dstate[h] = ds
            dke_c, dv_c, dqd_c, dtot_c = [], [], [], []
            for c in range(ncht):
                st0 = st_ref[c, h]
                dsb = ds1[c].astype(BF16)
                dke_ = _dot(v_[_rows(c)], dsb)
                dke_c.append(dke_)
                dv_c.append(_dot(ke_[_rows(c)], dsb, NT))
                dqd_c.append(_dot(do_[_rows(c)], st0.astype(BF16)))
                dt = (jnp.sum(ds1[c] * st0, axis=0, keepdims=True) * et[c * CH:c * CH + 1, sl]
                      + jnp.sum(dke_ * kef[_rows(c), sl], axis=0, keepdims=True))
                dtot_c.append(jnp.broadcast_to(dt, (CH, HGD)))
            dke = jnp.concatenate(dke_c, axis=0)
            dqd = dqd + jnp.concatenate(dqd_c, axis=0)
            dv_l.append(dv + jnp.concatenate(dv_c, axis=0))
            dtot_l.append(jnp.concatenate(dtot_c, axis=0))
            dq_l.append(dqd * ea[:, sl])
            dk_l.append(dkd * eb[:, sl] + dke * ee[:, sl])
            dcum_l.append(dqd * qdf[:, sl] - dkd * kdf[:, sl] - dke * kef[:, sl])
        dcum = jnp.concatenate(dcum_l, axis=1)
        dlf = _dot(causal_t.astype(F32), dcum, prec=HI) + jnp.concatenate(dtot_l, axis=1)
        dq_t = jnp.concatenate(dq_l, axis=1)
        dv_t = jnp.concatenate(dv_l, axis=1)

        df = dlf / f - jnp.concatenate(dk_l, axis=1)
        dfl = df * (1.0 - lb) * sg * (1.0 - sg)
        dlb = jnp.sum(df * (1.0 - sg), axis=0, keepdims=True)
        dl0 = dlb * lb * (1.0 - lb)
        dlg_ref[...] += jnp.concatenate([dl0, -dl0], axis=0)[None]
        if second:
            sq = _sig(qraw)
            dqr = (dqp_ref[...] + dq_t) * (HGD ** -0.5) * (sq * (1.0 + qraw * (1.0 - sq)))
            dp_ref[...] = jnp.concatenate([dfl, dvp_ref[...] + dv_t, dqr], axis=1).astype(BF16)
        else:
            dp_ref[...] = dfl.astype(BF16)
            dv_ref[...] = dv_t
            dq_ref[...] = dq_t

    def col(cb):
        return pl.BlockSpec((TM, HGW), lambda s: (tile_of(s), cb))

    tok = pl.BlockSpec((TM, HGW), lambda s: (tile_of(s), 0))
    in_specs = [col(C_FB if rev else C_FF), col(C_INP), col(C_QHG),
                pl.BlockSpec((1, 2, HGW), lambda s: (d, 0, 0)),
                pl.BlockSpec((TM, HGW), lambda s: (jnp.maximum(tile_of(s) - 1, 0), 0)),
                pl.BlockSpec((ncht, 4, HGD, HGD), lambda s: (tile_of(s), 0, 0, 0))]
    args = [p, p, p, lg, do, st]
    dlg_spec = _full((1, 2, HGW))
    dlg_shape = jax.ShapeDtypeStruct((1, 2, HGW), F32)
    if second:
        in_specs += [tok, tok]
        args += [prev[0], prev[1]]
        out_specs = [pl.BlockSpec((TM, 3 * HGW), lambda s: (tile_of(s), 0)), dlg_spec]
        out_shape = [jax.ShapeDtypeStruct(dp.shape, BF16), dlg_shape]
    else:
        out_specs = [pl.BlockSpec((TM, HGW), lambda s: (tile_of(s), C_FB if rev else C_FF)), tok, tok, dlg_spec]
        out_shape = [jax.ShapeDtypeStruct(dp.shape, BF16), jax.ShapeDtypeStruct((tt, HGW), F32),
                     jax.ShapeDtypeStruct((tt, HGW), F32), dlg_shape]
    in_specs.append(ANY)
    args.append(dp)
    return _pcall(body, name="hgrn_bwd_rev" if rev else "hgrn_bwd", grid=(nt,),
                  in_specs=in_specs, out_specs=out_specs, out_shape=out_shape,
                  scratch=[pltpu.VMEM((4, HGD, HGD), F32)],
                  aliases={len(args) - 1: 0}, carry=carry)(*args)


def _head_rms(o, w, nheads):
    outs = []
    for h in range(nheads):
        oh = o[:, h * HGD:(h + 1) * HGD]
        outs.append(oh * lax.rsqrt(jnp.mean(oh * oh, axis=-1, keepdims=True) + EPS))
    return jnp.concatenate(outs, axis=1)


def _readout(o0, o1, p, hw4):
    s_len = o0.shape[0] - L

    def body(o0_ref, o1_ref, g_ref, w_ref, y_ref):
        xh = _head_rms(o0_ref[...] + o1_ref[...], None, 4)
        g = g_ref[...]
        y_ref[...] = (xh * w_ref[...] * (g * _sig(g))).astype(BF16)

    lat = pl.BlockSpec((TM, HGW), lambda i: (i + 1, 0))
    return _pcall(body, name="readout", grid=(s_len // TM,),
                  in_specs=[lat, lat, pl.BlockSpec((TM, HGW), lambda i: (i + 1, C_GHG)), _full((1, HGW))],
                  out_specs=pl.BlockSpec((TM, HGW), lambda i: (i, 0)),
                  out_shape=jax.ShapeDtypeStruct((s_len, HGW), BF16))(o0, o1, p, hw4)


def _readout_bwd(o0, o1, p, hw4, dy, dp, carry=None):
    tt = o0.shape[0]
    s_len = tt - L

    def body(o0_ref, o1_ref, g_ref, w_ref, dy_ref, _dp_in, dp_ref, do_ref, dw_ref):
        i = pl.program_id(0)

        @pl.when(i == 0)
        def _():
            dw_ref[...] = jnp.zeros_like(dw_ref)
            dp_ref[...] = jnp.zeros_like(dp_ref)

        @pl.when(i >= 1)
        def _():
            o = o0_ref[...] + o1_ref[...]
            g = g_ref[...]
            w = w_ref[...]
            sg = _sig(g)
            dy_ = dy_ref[...]
            dsw = dy_ * (g * sg)
            outs, xhs = [], []
            for h in range(4):
                sl = slice(h * HGD, (h + 1) * HGD)
                oh = o[:, sl]
                r = lax.rsqrt(jnp.mean(oh * oh, axis=-1, keepdims=True) + EPS)
                xh = oh * r
                dxh = dsw[:, sl] * w[:, sl]
                outs.append(r * (dxh - xh * jnp.mean(dxh * xh, axis=-1, keepdims=True)))
                xhs.append(xh)
            xh = jnp.concatenate(xhs, axis=1)
            do_ref[...] = jnp.concatenate(outs, axis=1)
            dp_ref[...] = (dy_ * xh * w * (sg * (1.0 + g * (1.0 - sg)))).astype(BF16)
            dw_ref[...] += jnp.sum(dsw * xh, axis=0, keepdims=True)

    tok = pl.BlockSpec((TM, HGW), lambda i: (i, 0))
    lat = pl.BlockSpec((TM, HGW), lambda i: (jnp.maximum(i - 1, 0), 0))
    return _pcall(body, name="readout_bwd", grid=(tt // TM,),
                  in_specs=[tok, tok, pl.BlockSpec((TM, HGW), lambda i: (i, C_GHG)), _full((1, HGW)), lat, ANY],
                  out_specs=[pl.BlockSpec((TM, HGW), lambda i: (i, C_GHG)), lat, _full((1, HGW))],
                  out_shape=[jax.ShapeDtypeStruct(dp.shape, BF16), jax.ShapeDtypeStruct((s_len, HGW), F32),
                             jax.ShapeDtypeStruct((1, HGW), F32)],
                  aliases={5: 0}, carry=carry)(o0, o1, p, hw4, dy, dp)


def _rope_tables(s_len):
    t = np.arange(s_len)
    inv = ROPE_THETA ** (-np.arange(0, 32, 2, dtype=np.float64) / 32)
    def half(pos):
        ang = pos[:, None].astype(np.float64) * inv[None, :]
        return (np.concatenate([np.cos(ang), np.cos(ang)], 1), np.concatenate([-np.sin(ang), np.sin(ang)], 1))
    cr, sr = half(t // GRID_W)
    cc, sc = half(t % GRID_W)
    cos = np.concatenate([cr, cc, cr, cc], 1)
    sin = np.concatenate([sr, sc, sr, sc], 1)
    cos = np.concatenate([np.ones((L, 128)), cos], 0)
    sin = np.concatenate([np.zeros((L, 128)), sin], 0)
    return jnp.asarray(cos, F32), jnp.asarray(sin, F32)


def _blockdiag(n, w):
    i = np.arange(n)
    return jnp.asarray((i[:, None] // w == i[None, :] // w) / float(w), F32)


def _dup_matrix():
    m = np.zeros((128, 512), np.float32)
    for g in range(2):
        for j in range(4):
            for dd in range(HDIM):
                m[64 * g + dd, 256 * g + 64 * j + dd] = 1.0
    return m


def _rot(x):
    n = x.shape[1]
    lane = lax.broadcasted_iota(jnp.int32, x.shape, 1)
    return jnp.where((lane % 32) < 16, pltpu.roll(x, n - 16, 1), pltpu.roll(x, 16, 1))


def _qk_prep(p, cos, sin, qnw8, knw2, bd512, bd128, dup):
    tt = p.shape[0]

    def body(q_ref, kv_ref, cos_ref, sin_ref, qw_ref, kw_ref, b5_ref, b1_ref, dup_ref,
             qr_ref, k4_ref, v4_ref):
        cos_, sin_ = cos_ref[...], sin_ref[...]
        q = q_ref[...]
        qn = q * lax.rsqrt(_dot(q * q, b5_ref[...], prec=HI) + EPS) * qw_ref[...]
        cos4 = jnp.concatenate([cos_] * 4, axis=1)
        sin4 = jnp.concatenate([sin_] * 4, axis=1)
        qr_ref[...] = ((qn * cos4 + _rot(qn) * sin4) * (HDIM ** -0.5)).astype(BF16)
        kv = kv_ref[...]
        k, v = kv[:, :128], kv[:, 128:]
        kn = k * lax.rsqrt(_dot(k * k, b1_ref[...], prec=HI) + EPS) * kw_ref[...]
        kr = kn * cos_ + _rot(kn) * sin_
        k4_ref[...] = _bdot(kr, dup_ref[...]).astype(BF16)
        v4_ref[...] = _bdot(v, dup_ref[...]).astype(BF16)

    row = lambda w, cb: pl.BlockSpec((TM, w), lambda i: (i, cb))
    out = jax.ShapeDtypeStruct((tt, ATW), BF16)
    return _pcall(body, name="qk_prep", grid=(tt // TM,),
                  in_specs=[row(ATW, C_QRAW), row(256, C_KV), row(128, 0), row(128, 0),
                            _full((1, ATW)), _full((1, 128)), _full((ATW, ATW)), _full((128, 128)),
                            _full((128, ATW))],
                  out_specs=[row(ATW, 0)] * 3, out_shape=[out] * 3)(
                      p, p, cos, sin, qnw8, knw2, bd512, bd128, dup)


def _attn_masks(i, nb):
    r = lax.broadcasted_iota(jnp.int32, (4 * BLK, 3 * BLK + L), 0) % BLK
    c = lax.broadcasted_iota(jnp.int32, (4 * BLK, 3 * BLK + L), 1)
    kpos = (i - 1) * BLK + c
    loc = (jnp.abs(c - BLK - r) <= BLK) & (kpos >= 0) & (kpos < nb * BLK)
    return loc | (c >= 3 * BLK)


def _stack_mask():
    r = lax.broadcasted_iota(jnp.int32, (4 * BLK, 256), 0)
    lane = lax.broadcasted_iota(jnp.int32, (4 * BLK, 256), 1)
    return (r // BLK) == (lane // HDIM)


def _stack_heads(xg, fill=0.0):
    x4 = jnp.concatenate([xg] * 4, axis=0)
    return jnp.where(_stack_mask(), x4, jnp.full_like(x4, fill))


def _unstack_heads(x4):
    out = jnp.where(_lane_mask(0), x4[0:BLK], 0.0)
    for j in range(1, 4):
        out = out + jnp.where(_lane_mask(j), x4[j * BLK:(j + 1) * BLK], 0.0)
    return out


def _per_head_rows(vals):
    return jnp.concatenate([jnp.broadcast_to(v, (BLK, 1)) for v in vals], axis=0)


def _lane_mask(j):
    lane = lax.broadcasted_iota(jnp.int32, (1, 256), 1)
    return (lane // HDIM) == j


def _attn_specs(nb):
    blk = lambda off: pl.BlockSpec((BLK, ATW), lambda i: (jnp.clip(i + off, 0, nb - 1) + 2, 0))
    ctx = pl.BlockSpec((L, ATW), lambda i: (0, 0))
    return blk, ctx


def _attn_fwd(qr, k4, v4, sinks, carry=None):
    tt = qr.shape[0]
    s_len = tt - L
    nb = s_len // BLK

    def body(sk_ref, q_ref, kp, ko, kn, kc, vp, vo, vn, vc, y_ref, lse_ref):
        i = pl.program_id(0)
        valid = _attn_masks(i, nb)
        q = q_ref[...]
        ys, lses = [], []
        for g in range(2):
            gs = slice(256 * g, 256 * g + 256)
            kcat = jnp.concatenate([kp[:, gs], ko[:, gs], kn[:, gs], kc[:, gs]], axis=0)
            vcat = jnp.concatenate([vp[:, gs], vo[:, gs], vn[:, gs], vc[:, gs]], axis=0)
            sink = _per_head_rows([sk_ref[4 * g + j] for j in range(4)])
            s = jnp.where(valid, _dot(_stack_heads(q[:, gs]), kcat, NT), -1e30)
            m = jnp.maximum(jnp.max(s, axis=-1, keepdims=True), sink)
            e = jnp.exp(s - m)
            den = jnp.sum(e, axis=-1, keepdims=True) + jnp.exp(sink - m)
            ys.append(_unstack_heads(_bdot(e / den, vcat)))
            lses.append(_unstack_heads(jnp.broadcast_to(m + jnp.log(den), (4 * BLK, 256))))
        y_ref[...] = jnp.concatenate(ys, axis=1).astype(BF16)
        lse_ref[...] = jnp.concatenate(lses, axis=1)

    blk, ctx = _attn_specs(nb)
    out = pl.BlockSpec((BLK, ATW), lambda i: (i, 0))
    return _pcall(body, name="attn_fwd", grid=(nb,),
                  in_specs=[pl.BlockSpec(memory_space=pltpu.SMEM), blk(0),
                            blk(-1), blk(0), blk(1), ctx, blk(-1), blk(0), blk(1), ctx],
                  out_specs=[out, out],
                  out_shape=[jax.ShapeDtypeStruct((s_len, ATW), BF16),
                             jax.ShapeDtypeStruct((s_len, ATW), F32)], carry=carry)(
                      sinks, qr, k4, k4, k4, k4, v4, v4, v4, v4)


def _attn_bwd(qr, k4, v4, sinks, y, lse, dy, carry=None):
    tt = qr.shape[0]
    s_len = tt - L
    nb = s_len // BLK

    def body(sk_ref, q_ref, kp, ko, kn, kc, vp, vo, vn, vc, y_ref, lse_ref, dy_ref,
             dq_ref, dkw_ref, dvw_ref, dkc_ref, dvc_ref, dsk_ref):
        i = pl.program_id(0)

        @pl.when(i == 0)
        def _():
            dkc_ref[...] = jnp.zeros_like(dkc_ref)
            dvc_ref[...] = jnp.zeros_like(dvc_ref)
            dsk_ref[...] = jnp.zeros_like(dsk_ref)

        valid = _attn_masks(i, nb)
        q = q_ref[...]
        dy_ = dy_ref[...]
        dly = dy_ * y_ref[...].astype(F32)
        lse_ = lse_ref[...]
        dqs = []
        for g in range(2):
            gs = slice(256 * g, 256 * g + 256)
            kcat = jnp.concatenate([kp[:, gs], ko[:, gs], kn[:, gs], kc[:, gs]], axis=0)
            vcat = jnp.concatenate([vp[:, gs], vo[:, gs], vn[:, gs], vc[:, gs]], axis=0)
            q4 = _stack_heads(q[:, gs])
            dy4 = _stack_heads(dy_[:, gs]).astype(BF16)
            lse4 = jnp.max(_stack_heads(lse_[:, gs], fill=-1e30), axis=-1, keepdims=True)
            delta = jnp.sum(_stack_heads(dly[:, gs]), axis=-1, keepdims=True)
            sink = _per_head_rows([sk_ref[4 * g + j] for j in range(4)])
            pr = jnp.where(valid, jnp.exp(_dot(q4, kcat, NT) - lse4), 0.0)
            dsb = (pr * (_dot(dy4, vcat, NT) - delta)).astype(BF16)
            dsink = jnp.exp(sink - lse4) * delta
            for j in range(4):
                dsk_ref[4 * g + j:4 * g + j + 1, :] += jnp.broadcast_to(
                    -jnp.sum(dsink[j * BLK:(j + 1) * BLK], axis=0, keepdims=True), (1, 128))
            dqs.append(_unstack_heads(_dot(dsb, kcat)))
            dkg = _dot(dsb, q4, TN)
            dvg = _dot(pr.astype(BF16), dy4, TN)
            dkw_ref[0, :, gs] = dkg[:3 * BLK]
            dvw_ref[0, :, gs] = dvg[:3 * BLK]
            dkc_ref[:, gs] += dkg[3 * BLK:]
            dvc_ref[:, gs] += dvg[3 * BLK:]
        dq_ref[...] = jnp.concatenate(dqs, axis=1)

    blk, ctx = _attn_specs(nb)
    out = pl.BlockSpec((BLK, ATW), lambda i: (i, 0))
    win = pl.BlockSpec((1, 3 * BLK, ATW), lambda i: (i, 0, 0))
    acc = _full((L, ATW))
    return _pcall(body, name="attn_bwd", grid=(nb,),
                  in_specs=[pl.BlockSpec(memory_space=pltpu.SMEM), blk(0),
                            blk(-1), blk(0), blk(1), ctx, blk(-1), blk(0), blk(1), ctx, out, out, out],
                  out_specs=[out, win, win, acc, acc, _full((8, 128))],
                  out_shape=[jax.ShapeDtypeStruct((s_len, ATW), F32),
                             jax.ShapeDtypeStruct((nb, 3 * BLK, ATW), F32),
                             jax.ShapeDtypeStruct((nb, 3 * BLK, ATW), F32),
                             jax.ShapeDtypeStruct((L, ATW), F32), jax.ShapeDtypeStruct((L, ATW), F32),
                             jax.ShapeDtypeStruct((8, 128), F32)], carry=carry)(
                      sinks, qr, k4, k4, k4, k4, v4, v4, v4, v4, y, lse, dy)


def _attn_post(p, cos, sin, qnw8, knw2, bd512, bd128, dupt, dq, dkw, dvw, dkc, dvc, dp, carry=None):
    tt = p.shape[0]
    s_len = tt - L
    nb = s_len // BLK
    nctx = L // BLK

    def body(q_ref, kv_ref, cos_ref, sin_ref, qw_ref, kw_ref, b5_ref, b1_ref, dupt_ref,
             dq_ref, kwp, kwo, kwn, vwp, vwo, vwn, dkc_ref, dvc_ref, _dp_in,
             dp_ref, dqw_ref, dkw_ref):
        t = pl.program_id(0)
        j = t - nctx

        @pl.when(t == 0)
        def _():
            dqw_ref[...] = jnp.zeros_like(dqw_ref)
            dkw_ref[...] = jnp.zeros_like(dkw_ref)

        is_lat = t >= nctx
        cos_, sin_ = cos_ref[...], sin_ref[...]
        has_p = is_lat & (j >= 1)
        has_n = is_lat & (j <= nb - 2)
        dk4 = (jnp.where(is_lat, kwo[0], dkc_ref[...]) + jnp.where(has_p, kwp[0], 0.0)
               + jnp.where(has_n, kwn[0], 0.0))
        dv4 = (jnp.where(is_lat, vwo[0], dvc_ref[...]) + jnp.where(has_p, vwp[0], 0.0)
               + jnp.where(has_n, vwn[0], 0.0))
        dkr = _dot(dk4, dupt_ref[...], prec=HI)
        dv = _dot(dv4, dupt_ref[...], prec=HI)
        kv = kv_ref[...]
        k = kv[:, :128]
        kw = kw_ref[...]
        rk = lax.rsqrt(_dot(k * k, b1_ref[...], prec=HI) + EPS)
        xk = k * rk
        dkn = dkr * cos_ + _rot(dkr * sin_)
        dxk = dkn * kw
        dk = rk * (dxk - xk * _dot(dxk * xk, b1_ref[...], prec=HI))
        dkw_ref[...] += jnp.sum(dkn * xk, axis=0, keepdims=True)
        q = q_ref[...]
        qw = qw_ref[...]
        rq = lax.rsqrt(_dot(q * q, b5_ref[...], prec=HI) + EPS)
        xq = q * rq
        cos4 = jnp.concatenate([cos_] * 4, axis=1)
        sin4 = jnp.concatenate([sin_] * 4, axis=1)
        dqr = jnp.where(is_lat, dq_ref[...], 0.0) * (HDIM ** -0.5)
        dqn = dqr * cos4 + _rot(dqr * sin4)
        dxq = dqn * qw
        dqraw = rq * (dxq - xq * _dot(dxq * xq, b5_ref[...], prec=HI))
        dqw_ref[...] += jnp.sum(dqn * xq, axis=0, keepdims=True)
        dp_ref[...] = jnp.concatenate([dqraw, dk, dv], axis=1).astype(BF16)

    row = lambda w, cb: pl.BlockSpec((BLK, w), lambda t: (t, cb))
    lat = pl.BlockSpec((BLK, ATW), lambda t: (jnp.maximum(t - nctx, 0), 0))

    def part(off):
        return pl.BlockSpec((1, BLK, ATW), lambda t: (jnp.clip(t - nctx + off, 0, nb - 1), 1 - off, 0))

    cacc = pl.BlockSpec((BLK, ATW), lambda t: (jnp.minimum(t, nctx - 1), 0))
    return _pcall(body, name="attn_post", grid=(tt // BLK,),
                  in_specs=[row(ATW, C_QRAW), row(256, C_KV), row(128, 0), row(128, 0),
                            _full((1, ATW)), _full((1, 128)), _full((ATW, ATW)), _full((128, 128)),
                            _full((ATW, 128)), lat, part(-1), part(0), part(1), part(-1), part(0), part(1),
                            cacc, cacc, ANY],
                  out_specs=[pl.BlockSpec((BLK, 768), lambda t: (t, C_QKV)), _full((1, ATW)), _full((1, 128))],
                  out_shape=[jax.ShapeDtypeStruct(dp.shape, BF16), jax.ShapeDtypeStruct((1, ATW), F32),
                             jax.ShapeDtypeStruct((1, 128), F32)],
                  aliases={18: 0}, carry=carry)(p, p, cos, sin, qnw8, knw2, bd512, bd128, dupt,
                                   dq, dkw, dkw, dkw, dvw, dvw, dvw, dkc, dvc, dp)


def _merge(ah, aa, p):
    s_len = ah.shape[0]

    def body(ah_ref, aa_ref, gh_ref, ga_ref, m_ref):
        m_ref[...] = (_sig(gh_ref[...]) * ah_ref[...] + _sig(ga_ref[...]) * aa_ref[...]).astype(BF16)

    row = pl.BlockSpec((TM, D), lambda i: (i, 0))
    return _pcall(body, name="merge", grid=(s_len // TM,),
                  in_specs=[row, row, pl.BlockSpec((TM, D), lambda i: (i + 1, 2)),
                            pl.BlockSpec((TM, D), lambda i: (i + 1, 3))],
                  out_specs=row, out_shape=jax.ShapeDtypeStruct((s_len, D), BF16))(ah, aa, p, p)


def _merge_bwd(dm, ah, aa, p, carry=None):
    tt = p.shape[0]
    s_len = tt - L

    def body(dm_ref, ah_ref, aa_ref, gh_ref, ga_ref, dp_ref, dmh_ref, dma_ref):
        i = pl.program_id(0)

        @pl.when(i == 0)
        def _():
            dp_ref[...] = jnp.zeros_like(dp_ref)

        @pl.when(i >= 1)
        def _():
            dm_ = dm_ref[...]
            sh, sa = _sig(gh_ref[...]), _sig(ga_ref[...])
            dp_ref[...] = jnp.concatenate([dm_ * ah_ref[...] * sh * (1.0 - sh),
                                           dm_ * aa_ref[...] * sa * (1.0 - sa)], axis=1).astype(BF16)
            dmh_ref[...] = (dm_ * sh).astype(BF16)
            dma_ref[...] = (dm_ * sa).astype(BF16)

    lat = pl.BlockSpec((TM, D), lambda i: (jnp.maximum(i - 1, 0), 0))
    return _pcall(body, name="merge_bwd", grid=(tt // TM,),
                  in_specs=[lat, lat, lat, pl.BlockSpec((TM, D), lambda i: (i, 2)),
                            pl.BlockSpec((TM, D), lambda i: (i, 3))],
                  out_specs=[pl.BlockSpec((TM, 2 * D), lambda i: (i, C_GATES)), lat, lat],
                  out_shape=[jax.ShapeDtypeStruct((tt, NCOL), BF16), jax.ShapeDtypeStruct((s_len, D), BF16),
                             jax.ShapeDtypeStruct((s_len, D), BF16)], carry=carry)(dm, ah, aa, p, p)


def _local_step(x, ctx, tgt, mod, modc, nw1, nw2, lg, hw, qnw, knw, sinks,
                w_in, wts, dist=None):
    s_len = x.shape[0]
    tt = s_len + L
    tok = jnp.concatenate([ctx, x], axis=0)
    ss1 = jnp.stack([modc, mod[0:2]])
    ss2 = mod[3:5][None]
    g1, g2 = mod[2:3], mod[5:6]
    hw4 = jnp.tile(hw, (1, 4))
    qnw8 = jnp.tile(qnw, (1, 8))
    knw2 = jnp.tile(knw, (1, 2))
    cos, sin = _rope_tables(s_len)
    bd512, bd128 = _blockdiag(ATW, HDIM), _blockdiag(128, HDIM)
    dupm = _dup_matrix()
    dup, dupt = jnp.asarray(dupm, BF16), jnp.asarray(dupm.T, F32)
    tmt = tt

    def four(b):
        return b.reshape(4, 2 * b.shape[1], b.shape[2])

    def halves(g):
        return g.reshape(4, 2, g.shape[1] // 2, g.shape[2])

    h = _modulate(tok, nw1, ss1, name="mod1", sel=lambda i: jnp.minimum(i, 1))
    if dist is None:
        bh4, ba4, w_o, g4, u4, dn4 = wts
        p = _mm_in(h, w_in, tmt)
        o0, st0 = _hgrn_fwd(p, lg, rev=False)
        o1, st1 = _hgrn_fwd(p, lg, rev=True)
    else:
        core, chip = dist
        p, first = _mm_in(h, w_in, tmt, carry=_carry_gather(list(wts[0:3])))
        (o0, st0), (g8,) = _hgrn_fwd(p, lg, rev=False, carry=_carry_gather([wts[3]]))
        o1, st1 = _hgrn_fwd(p, lg, rev=True)
        bh4, ba4, w_o, g4 = four(first[0]), four(first[1]), four(first[2]).reshape(D, D), four(g8)
    y_hg = _readout(o0, o1, p, hw4)
    qr, k4, v4 = _qk_prep(p, cos, sin, qnw8, knw2, bd512, bd128, dup)
    if dist is None:
        y_at, lse = _attn_fwd(qr, k4, v4, sinks)
    else:
        (y_at, lse), (u8, dn8) = _attn_fwd(qr, k4, v4, sinks, carry=_carry_gather(list(wts[4:6])))
        u4, dn4 = four(u8), four(dn8)
    ah = _mm_cs(y_hg, bh4, name="mm_bh")
    aa = _mm_cs(y_at, ba4, name="mm_ba")
    mixed = _merge(ah, aa, p)
    ao = _mm(mixed, w_o, name="mm_o", tm=512, tn=D, tk=D)
    x1, h2 = _res1_mod2(x, ao, g1, nw2, ss2)
    a4, b4, z4 = _ffn_up(h2, g4, u4)
    y = _ffn_down(z4, dn4)
    sq, dx2, dyb, dg2 = _loss_head(x1, y, g2, tgt)

    da4, db4 = _ffn_dz(dyb, dn4, a4, b4)
    g_dn = _ffn_gdn(z4, dyb)
    dh2 = _ffn_dh2(da4, db4, g4, u4)
    g_g, g_u = _ffn_ggu(h2, da4, db4)
    dx1, dattn, dss2, dnw2, dg1 = _mod2_bwd(x1, dh2, dx2, ao, nw2, ss2, g1)
    dm = _mm(dattn, w_o, name="mm_dm", mode="nt", tm=512, tn=D, tk=D)
    g_o = _mm(mixed, dattn, name="mm_go", mode="tn", tm=D, tn=D, tk=512)
    if dist is None:
        dp, dmh, dma = _merge_bwd(dm, ah, aa, p)
    else:
        ffn_units = [halves(g_dn), halves(g_g), halves(g_u)]
        (dp, dmh, dma), ffn_recv = _merge_bwd(dm, ah, aa, p, carry=_carry_pairx(ffn_units))
        ffn_pairs = _rs_pair_add(ffn_units, ffn_recv, core)
    dy_hg = _mm_cs_nt(dmh, bh4, name="mm_dyh")
    dy_at = _mm_cs_nt(dma, ba4, name="mm_dya")
    g_bh = _mm_cs_tn(y_hg, dmh, D // 4, name="mm_gbh")
    g_ba = _mm_cs_tn(y_at, dma, D // 4, name="mm_gba")
    if dist is None:
        dp, do, dhw4 = _readout_bwd(o0, o1, p, hw4, dy_hg, dp)
        dq, dkw, dvw, dkc, dvc, dsk = _attn_bwd(qr, k4, v4, sinks, y_at, lse, dy_at)
        dp, dqnw8, dknw2 = _attn_post(p, cos, sin, qnw8, knw2, bd512, bd128, dupt, dq, dkw, dvw, dkc, dvc, dp)
    else:
        mix_units = [halves(g_bh), halves(g_ba), halves(g_o.reshape(4, D // 4, D))]
        (dp, do, dhw4), mix_recv = _readout_bwd(o0, o1, p, hw4, dy_hg, dp, carry=_carry_pairx(mix_units))
        mix_pairs = _rs_pair_add(mix_units, mix_recv, core)
        (dq, dkw, dvw, dkc, dvc, dsk), contribs_a = _attn_bwd(qr, k4, v4, sinks, y_at, lse, dy_at,
                                                              carry=_carry_chipx(ffn_pairs[0:2]))
        reds_a = _rs_chip_add(ffn_pairs[0:2], contribs_a, core, chip)
        (dp, dqnw8, dknw2), post = _attn_post(
            p, cos, sin, qnw8, knw2, bd512, bd128, dupt, dq, dkw, dvw, dkc, dvc, dp,
            carry=_carry_join(_carry_chipx(ffn_pairs[2:3]), _carry_sibx(reds_a)))
        reds_b = _rs_chip_add(ffn_pairs[2:3], post[0:1], core, chip)
    if dist is None:
        dp, dv0, dq0, dlg0 = _hgrn_bwd(p, lg, do, st0, dp, None, rev=False)
        dp, dlg1 = _hgrn_bwd(p, lg, do, st1, dp, (dv0, dq0), rev=True)
    else:
        (dp, dv0, dq0, dlg0), mid = _hgrn_bwd(p, lg, do, st0, dp, None, rev=False,
                                              carry=_carry_join(_carry_chipx(mix_pairs), _carry_sibx(reds_b)))
        mix_reds = _rs_chip_add(mix_pairs, mid[0:3], core, chip)
        (dp, dlg1), mix_done = _hgrn_bwd(p, lg, do, st1, dp, (dv0, dq0), rev=True, carry=_carry_sibx(mix_reds))
        ffn_done = post[1:3] + mid[3:4]
    dh = _mm_dh(dp, w_in, tmt)
    g_in = _mm_gin(dp, h, tmt)
    if dist is None:
        gx, dss1, dnw1 = _mod1_bwd(tok, dh, dx1, nw1, ss1)
        rs = None
    else:
        in_units = [halves(g_in.reshape(4, NCOL // 4, D))]
        (gx, dss1, dnw1), in_recv = _mod1_bwd(tok, dh, dx1, nw1, ss1, carry=_carry_pairx(in_units))
        rs = dict(ffn_done=ffn_done, mix_done=mix_done, in_units=in_units, in_recv=in_recv)

    dmod = jnp.concatenate([dss1[1], dg1, dss2, dg2], axis=0)
    dmodc = dss1[0]
    raw = (dss1, dg1, dss2, dg2, dnw1, dnw2, dhw4, dqnw8, dknw2, dsk, dlg0, dlg1)
    small = dict(raw=raw, dmod=dmod, dmodc=dmodc, dnw1=dnw1, dnw2=dnw2,
                 dhw=dhw4.reshape(4, HGD).sum(0, keepdims=True),
                 dqnw=dqnw8.reshape(8, HDIM).sum(0, keepdims=True),
                 dknw=dknw2.reshape(2, HDIM).sum(0, keepdims=True),
                 dsinks=dsk[:, 0], dlg=jnp.concatenate([dlg0, dlg1], axis=0))
    big = dict(w_in=g_in, w_bh=g_bh, w_ba=g_ba, w_o=g_o, w_g=g_g, w_u=g_u, w_dn=g_dn)
    return sq, gx, big, small, rs


def _place():
    x, y, c = lax.axis_index("x"), lax.axis_index("y"), lax.axis_index("c")
    return x, y, c


def _gather_blocks(x_refs, out_refs, send_sems, recv_sems, local_sems):
    n = len(out_refs)
    x, y, c = _place()
    me, sibling = (x, y, c), (x, y, 1 - c)
    chips = [(1 - x, y), (x, 1 - y), (1 - x, 1 - y)]

    def slot(u, px, py, pc):
        return out_refs[u].at[4 * px + 2 * py + pc]

    def copy(u, k, block, to, src=None):
        return pltpu.make_async_remote_copy(
            src_ref=slot(u, *block) if src is None else src, dst_ref=slot(u, *block),
            send_sem=send_sems.at[u, k], recv_sem=recv_sems.at[u, k], device_id=to, device_id_type=MESH)

    mines = []
    if x_refs is not None:
        mines = [pltpu.make_async_copy(x_refs[u], slot(u, *me), local_sems.at[u]) for u in range(n)]
    for cp in mines:
        cp.start()
    first = []
    for u in range(n):
        src = None if x_refs is None else x_refs[u]
        first.append(copy(u, 0, me, sibling, src=src))
        first += [copy(u, 1 + j, me, (*chip, c), src=src) for j, chip in enumerate(chips)]
    for cp in first:
        cp.start()
    passed = []
    for j, chip in enumerate(chips):
        for u in range(n):
            copy(u, 1 + j, (*chip, c), me).wait_recv()
            fwd = copy(u, 4 + j, (*chip, c), sibling)
            fwd.start()
            passed.append(fwd)
    for u in range(n):
        copy(u, 0, sibling, me).wait_recv()
    for j, chip in enumerate(chips):
        for u in range(n):
            copy(u, 4 + j, (*chip, 1 - c), me).wait_recv()
    for cp in first + passed:
        cp.wait_send()
    for cp in mines:
        cp.wait()


def _gather_sems(n):
    return [pltpu.SemaphoreType.DMA((n, 7)), pltpu.SemaphoreType.DMA((n, 7)), pltpu.SemaphoreType.DMA((n,))]


def _allgather(blks, *, name, in_vmem):
    n = len(blks)
    space = pltpu.VMEM if in_vmem else pl.ANY

    def body(*refs):
        _gather_blocks(refs[:n], refs[n:2 * n], *refs[2 * n:])

    return pl.pallas_call(
        body, name=name, out_shape=[jax.ShapeDtypeStruct((8,) + b.shape, b.dtype) for b in blks],
        in_specs=[pl.BlockSpec(memory_space=space)] * n, out_specs=[pl.BlockSpec(memory_space=space)] * n,
        scratch_shapes=_gather_sems(n))(*blks)


def _cast_place(ws, c, dev):
    n = len(ws)

    def body(s_ref, *refs):
        for u in range(n):
            refs[n + u][0] = refs[u][...].astype(BF16)

    in_specs, out_specs, out_shape = [], [], []
    for w in ws:
        q, cols = w.shape[0] // 4, w.shape[1]
        in_specs.append(pl.BlockSpec((q, cols), lambda i, s: (2 * s[0] + i, 0)))
        out_specs.append(pl.BlockSpec((1, q, cols), lambda i, s: (s[1], i, 0)))
        out_shape.append(jax.ShapeDtypeStruct((8, 2 * q, cols), BF16))
    return pl.pallas_call(
        body, name="cast_place",
        grid_spec=pltpu.PrefetchScalarGridSpec(num_scalar_prefetch=1, grid=(2,), in_specs=in_specs,
                                               out_specs=out_specs),
        out_shape=out_shape,
        compiler_params=pltpu.CompilerParams(vmem_limit_bytes=48 << 20))(jnp.stack([c, dev]), *ws)


def _gather_phases(out_refs, send_sems, recv_sems):
    n = len(out_refs)
    x, y, c = _place()
    me, sibling = (x, y, c), (x, y, 1 - c)
    chips = [(1 - x, y), (x, 1 - y), (1 - x, 1 - y)]

    def copy(u, k, block, to):
        px, py, pc = block
        ref = out_refs[u].at[4 * px + 2 * py + pc]
        return pltpu.make_async_remote_copy(src_ref=ref, dst_ref=ref, send_sem=send_sems.at[u, k],
                                            recv_sem=recv_sems.at[u, k], device_id=to, device_id_type=MESH)

    def start():
        for u in range(n):
            copy(u, 0, me, sibling).start()
            for j, chip in enumerate(chips):
                copy(u, 1 + j, me, (*chip, c)).start()

    def mid():
        for j, chip in enumerate(chips):
            for u in range(n):
                copy(u, 1 + j, (*chip, c), me).wait_recv()
                copy(u, 4 + j, (*chip, c), sibling).start()

    def end():
        for u in range(n):
            copy(u, 0, sibling, me).wait_recv()
        for j, chip in enumerate(chips):
            for u in range(n):
                copy(u, 4 + j, (*chip, 1 - c), me).wait_recv()
        for u in range(n):
            copy(u, 0, me, sibling).wait_send()
            for j, chip in enumerate(chips):
                copy(u, 1 + j, me, (*chip, c)).wait_send()
                copy(u, 4 + j, (*chip, c), sibling).wait_send()

    return start, mid, end


def _carry_gather(bufs):
    n = len(bufs)
    return _Carry(bufs, [jax.ShapeDtypeStruct(b.shape, b.dtype) for b in bufs], {u: u for u in range(n)},
                  [pltpu.SemaphoreType.DMA((n, 7)), pltpu.SemaphoreType.DMA((n, 7))],
                  lambda ins, outs, sems: _gather_phases(outs, *sems))


def _allgather_inplace(bufs, *, name):
    n = len(bufs)

    def body(*refs):
        for phase in _gather_phases(refs[n:2 * n], *refs[2 * n:]):
            phase()

    return pl.pallas_call(
        body, name=name, out_shape=[jax.ShapeDtypeStruct(b.shape, b.dtype) for b in bufs],
        in_specs=[ANY] * n, out_specs=[ANY] * n, input_output_aliases={u: u for u in range(n)},
        scratch_shapes=[pltpu.SemaphoreType.DMA((n, 7)), pltpu.SemaphoreType.DMA((n, 7))])(*bufs)


def _ag_small(raw):
    def body(dss1, dg1, dss2, dg2, dnw1, dnw2, dhw4, dqnw8, dknw2, dsk, dlg0, dlg1,
             out_ref, tot_ref, blk, send_sems, recv_sems, local_sems):
        blk[...] = jnp.zeros_like(blk)
        blk[0:2, :] = dss1[1]
        blk[2:3, :] = dg1[...]
        blk[3:5, :] = dss2[...]
        blk[5:6, :] = dg2[...]
        blk[6:8, :] = dss1[0]
        blk[8:9, :] = dnw1[...]
        blk[9:10, :] = dnw2[...]
        blk[10:11, 0:HGW] = dhw4[...]
        blk[10:11, HGW:D] = dqnw8[...]
        blk[11:12, 0:128] = dknw2[...]
        blk[12:14, 0:HGW] = dlg0[0]
        blk[14:16, 0:HGW] = dlg1[0]
        blk[16:24, 0:128] = dsk[...]
        _gather_blocks([blk], [out_ref], send_sems, recv_sems, local_sems)
        acc = out_ref[0]
        for i in range(1, 8):
            acc = acc + out_ref[i]
        tot_ref[...] = acc

    vm = pl.BlockSpec(memory_space=pltpu.VMEM)
    return pl.pallas_call(
        body, name="ag_small",
        out_shape=[jax.ShapeDtypeStruct((8, 24, D), F32), jax.ShapeDtypeStruct((24, D), F32)],
        in_specs=[vm] * 12, out_specs=[vm, vm],
        scratch_shapes=[pltpu.VMEM((24, D), F32)] + _gather_sems(1))(*raw)


def _rs_pair_exchange(units):
    n = len(units)

    def body(*refs):
        start, _, end = _pairx_phases(refs[:n], refs[n:2 * n], *refs[2 * n:])
        start()
        end()

    return pl.pallas_call(
        body, name="rs_pair_exchange", out_shape=_pairx_shapes(units),
        in_specs=[ANY] * n, out_specs=[ANY] * n,
        scratch_shapes=[pltpu.SemaphoreType.DMA((n, 4)), pltpu.SemaphoreType.DMA((n, 4))])(*units)


def _pairx_shapes(units):
    return [jax.ShapeDtypeStruct((4,) + g.shape[2:], g.dtype) for g in units]


def _pairx_phases(g_refs, r_refs, send_sems, recv_sems):
    n = len(g_refs)
    x, y, c = _place()
    cps = [pltpu.make_async_remote_copy(
        src_ref=g_refs[u].at[j, 1 - c], dst_ref=r_refs[u].at[j], send_sem=send_sems.at[u, j],
        recv_sem=recv_sems.at[u, j], device_id=(x, y, 1 - c), device_id_type=MESH)
        for u in range(n) for j in range(4)]

    def start():
        for cp in cps:
            cp.start()

    def end():
        for cp in cps:
            cp.wait()

    return start, None, end


def _carry_pairx(units):
    n = len(units)
    return _Carry(units, _pairx_shapes(units), {},
                  [pltpu.SemaphoreType.DMA((n, 4)), pltpu.SemaphoreType.DMA((n, 4))],
                  lambda ins, outs, sems: _pairx_phases(ins, outs, *sems))


def _rs_pair_add(units, recvs, c):
    n = len(units)

    def body(c_ref, *refs):
        for u in range(n):
            refs[2 * n + u][...] = (refs[u][0] + refs[n + u][...]).astype(BF16)

    in_specs, out_specs, out_shape = [], [], []
    for g in units:
        h, w = g.shape[2] // 2, g.shape[3]
        in_specs.append(pl.BlockSpec((1, 1, h, w), lambda j, i, cr: (j, cr[0], i, 0)))
    for g in units:
        h, w = g.shape[2] // 2, g.shape[3]
        in_specs.append(pl.BlockSpec((1, h, w), lambda j, i, cr: (j, i, 0)))
        out_specs.append(pl.BlockSpec((1, h, w), lambda j, i, cr: (j, i, 0)))
        out_shape.append(jax.ShapeDtypeStruct((4, 2 * h, w), BF16))
    return pl.pallas_call(
        body, name="rs_pair_add",
        grid_spec=pltpu.PrefetchScalarGridSpec(num_scalar_prefetch=1, grid=(4, 2), in_specs=in_specs,
                                               out_specs=out_specs),
        out_shape=out_shape,
        compiler_params=pltpu.CompilerParams(vmem_limit_bytes=48 << 20))(c.reshape(1), *units, *recvs)


def _rs_chip_exchange(pairs):
    n = len(pairs)

    def body(*refs):
        start, _, end = _chipx_phases(refs[:n], refs[n:2 * n], *refs[2 * n:])
        start()
        end()

    return pl.pallas_call(
        body, name="rs_chip_exchange", out_shape=[jax.ShapeDtypeStruct(p.shape, p.dtype) for p in pairs],
        in_specs=[ANY] * n, out_specs=[ANY] * n,
        scratch_shapes=[pltpu.SemaphoreType.DMA((n, 3)), pltpu.SemaphoreType.DMA((n, 3))])(*pairs)


def _chipx_phases(p_refs, r_refs, send_sems, recv_sems):
    n = len(p_refs)
    x, y, c = _place()
    k = 2 * x + y
    sends = []
    for d in range(1, 4):
        j = (k + d) % 4
        for u in range(n):
            sends.append(pltpu.make_async_remote_copy(
                src_ref=p_refs[u].at[j], dst_ref=r_refs[u].at[k], send_sem=send_sems.at[u, d - 1],
                recv_sem=recv_sems.at[u, d - 1], device_id=(j // 2, j % 2, c), device_id_type=MESH))

    def start():
        for cp in sends:
            cp.start()

    def end():
        for d in range(1, 4):
            src = (k + 4 - d) % 4
            for u in range(n):
                pltpu.make_async_remote_copy(
                    src_ref=p_refs[u].at[src], dst_ref=r_refs[u].at[src], send_sem=send_sems.at[u, d - 1],
                    recv_sem=recv_sems.at[u, d - 1], device_id=(x, y, c), device_id_type=MESH).wait_recv()
        for cp in sends:
            cp.wait_send()

    return start, None, end


def _carry_chipx(pairs):
    n = len(pairs)
    return _Carry(pairs, [jax.ShapeDtypeStruct(p.shape, p.dtype) for p in pairs], {},
                  [pltpu.SemaphoreType.DMA((n, 3)), pltpu.SemaphoreType.DMA((n, 3))],
                  lambda ins, outs, sems: _chipx_phases(ins, outs, *sems))


def _rs_chip_add(pairs, contribs, c, chip):
    n = len(pairs)

    def body(s_ref, *refs):
        for u in range(n):
            a, b, c_, d = refs[4 * u:4 * u + 4]
            refs[4 * n + u][0] = ((a[0].astype(F32) + b[0].astype(F32)) + c_[0].astype(F32)) + d[0].astype(F32)

    in_specs, out_specs, out_shape, args = [], [], [], []
    for p, r in zip(pairs, contribs):
        h, w = p.shape[1] // 2, p.shape[2]
        in_specs += [pl.BlockSpec((1, h, w), functools.partial(lambda d, i, s: ((s[1] + d) % 4, i, 0), d))
                     for d in range(4)]
        args += [p, r, r, r]
        out_specs.append(pl.BlockSpec((1, h, w), lambda i, s: (s[0], i, 0)))
        out_shape.append(jax.ShapeDtypeStruct((2, 2 * h, w), F32))
    return pl.pallas_call(
        body, name="rs_chip_add",
        grid_spec=pltpu.PrefetchScalarGridSpec(num_scalar_prefetch=1, grid=(2,), in_specs=in_specs,
                                               out_specs=out_specs),
        out_shape=out_shape,
        compiler_params=pltpu.CompilerParams(vmem_limit_bytes=48 << 20))(jnp.stack([c, chip]), *args)


def _rs_sibling_gather(reds):
    n = len(reds)

    def body(*refs):
        start, _, end = _sibx_phases(refs[n:2 * n], *refs[2 * n:])
        start()
        end()

    return pl.pallas_call(
        body, name="rs_sibling_gather", out_shape=[jax.ShapeDtypeStruct(r.shape, r.dtype) for r in reds],
        in_specs=[ANY] * n, out_specs=[ANY] * n, input_output_aliases={u: u for u in range(n)},
        scratch_shapes=[pltpu.SemaphoreType.DMA((n,))] * 2)(*reds)


def _sibx_phases(o_refs, send_sems, recv_sems):
    n = len(o_refs)
    x, y, c = _place()
    cps = [pltpu.make_async_remote_copy(
        src_ref=o_refs[u].at[c], dst_ref=o_refs[u].at[c], send_sem=send_sems.at[u], recv_sem=recv_sems.at[u],
        device_id=(x, y, 1 - c), device_id_type=MESH) for u in range(n)]

    def start():
        for cp in cps:
            cp.start()

    def end():
        for u in range(n):
            cps[u].wait_send()
            pltpu.make_async_remote_copy(
                src_ref=o_refs[u].at[1 - c], dst_ref=o_refs[u].at[1 - c], send_sem=send_sems.at[u],
                recv_sem=recv_sems.at[u], device_id=(x, y, 1 - c), device_id_type=MESH).wait_recv()

    return start, None, end


def _carry_sibx(reds):
    n = len(reds)
    return _Carry(reds, [jax.ShapeDtypeStruct(r.shape, r.dtype) for r in reds], {u: u for u in range(n)},
                  [pltpu.SemaphoreType.DMA((n,))] * 2, lambda ins, outs, sems: _sibx_phases(outs, *sems))


def _ada_fwd(c16, w, b):
    n = w.shape[1]
    tn = 512

    def body(c_ref, w_ref, b_ref, o_ref):
        cc = c_ref[...]
        o_ref[...] = _dot(cc * _sig(cc), w_ref[...], prec=HI) + b_ref[...]

    return _pcall(body, name="ada_fwd", grid=(n // tn,),
                  in_specs=[_full((16, D)), pl.BlockSpec((D, tn), lambda j: (0, j)),
                            pl.BlockSpec((1, tn), lambda j: (0, j))],
                  out_specs=pl.BlockSpec((16, tn), lambda j: (0, j)),
                  out_shape=jax.ShapeDtypeStruct((16, n), F32))(c16, w, b)


def _ada_bwd(c16, dmod16, w):
    n = w.shape[1]
    tn = 512

    def body(c_ref, d_ref, w_ref, gw_ref, gc_ref):
        j = pl.program_id(0)

        @pl.when(j == 0)
        def _():
            gc_ref[...] = jnp.zeros_like(gc_ref)

        cc = c_ref[...]
        dm = d_ref[...]
        gw_ref[...] = _dot(cc * _sig(cc), dm, TN, prec=HI)
        gc_ref[...] += _dot(dm, w_ref[...], NT, prec=HI)

    return _pcall(body, name="ada_bwd", grid=(n // tn,),
                  in_specs=[_full((16, D)), pl.BlockSpec((16, tn), lambda j: (0, j)),
                            pl.BlockSpec((D, tn), lambda j: (0, j))],
                  out_specs=[pl.BlockSpec((D, tn), lambda j: (0, j)), _full((16, D))],
                  out_shape=[jax.ShapeDtypeStruct((D, n), F32),
                             jax.ShapeDtypeStruct((16, D), F32)])(c16, dmod16, w)


def _adam_math(w, g, m, v):
    c1 = 1.0 - ADAM_B1 ** ADAM_STEP
    c2 = 1.0 - ADAM_B2 ** ADAM_STEP
    nm = ADAM_B1 * m + (1.0 - ADAM_B1) * g
    nv = ADAM_B2 * v + (1.0 - ADAM_B2) * (g * g)
    return -ADAM_LR * ((nm / c1) / (jnp.sqrt(nv / c2) + ADAM_EPS) + ADAM_WD * w), nm, nv


def _adamw_small(ws, gs, ms, vs):
    n = len(ws)

    def body(*refs):
        for u in range(n):
            d_, nm, nv = _adam_math(refs[u][...], refs[n + u][...], refs[2 * n + u][...], refs[3 * n + u][...])
            refs[4 * n + u][...] = d_
            refs[5 * n + u][...] = nm
            refs[6 * n + u][...] = nv

    specs = [_full(w.shape) for w in ws]
    shapes = [jax.ShapeDtypeStruct(w.shape, F32) for w in ws]
    out = _pcall(body, name="adamw_small", grid=(1,), in_specs=specs * 4, out_specs=specs * 3,
                 out_shape=shapes * 3)(*ws, *gs, *ms, *vs)
    return out[:n], out[n:2 * n], out[2 * n:]


def _cctx_grad(parts, c_ctx):
    def body(p_ref, c_ref, o_ref):
        acc = p_ref[0:1, :]
        for k in range(1, 4):
            acc = acc + p_ref[k:k + 1, :]
        cc = c_ref[...]
        s = _sig(cc)
        o_ref[...] = acc * (s * (1.0 + cc * (1.0 - s)))

    return _pcall(body, name="cctx_grad", grid=(1,), in_specs=[_full(parts.shape), _full((1, D))],
                  out_specs=_full((1, D)), out_shape=jax.ShapeDtypeStruct((1, D), F32))(parts, c_ctx)


ADAM_STEPS = 8


def _adamw_multi(ws, gs, ms, vs, *, name, carry=None):
    n = len(ws)

    def body(*refs):
        for u in range(n):
            refs[4 * n + u][...], refs[5 * n + u][...], refs[6 * n + u][...] = _adam_math(
                refs[u][...], refs[n + u][...], refs[2 * n + u][...], refs[3 * n + u][...])

    specs = [pl.BlockSpec((w.shape[0] // ADAM_STEPS, w.shape[1]), lambda i: (i, 0)) for w in ws]
    shapes = [jax.ShapeDtypeStruct(w.shape, F32) for w in ws]
    res = _pcall(body, name=name, grid=(ADAM_STEPS,), in_specs=specs * 4, out_specs=specs * 3,
                 out_shape=shapes * 3, carry=carry)(*ws, *gs, *ms, *vs)
    out, extra = res if carry is not None else (res, None)
    return (out[:n], out[n:2 * n], out[2 * n:]), extra


def kernel(x, c, ctx, c_ctx, w_ada, b_ada, norm_mix_w, norm_ffn_w, w_in, hgrn_lb_logits, hgrn_norm_w, q_norm_w, k_norm_w, attn_sinks, w_branch_hgrn, w_branch_attn, w_out, w_ffn_gate, w_ffn_up, w_ffn_down, loss_target, m_c_ctx, m_w_ada, m_b_ada, m_norm_mix_w, m_norm_ffn_w, m_w_in, m_hgrn_lb_logits, m_hgrn_norm_w, m_q_norm_w, m_k_norm_w, m_attn_sinks, m_w_branch_hgrn, m_w_branch_attn, m_w_out, m_w_ffn_gate, m_w_ffn_up, m_w_ffn_down, v_c_ctx, v_w_ada, v_b_ada, v_norm_mix_w, v_norm_ffn_w, v_w_in, v_hgrn_lb_logits, v_hgrn_norm_w, v_q_norm_w, v_k_norm_w, v_attn_sinks, v_w_branch_hgrn, v_w_branch_attn, v_w_out, v_w_ffn_gate, v_w_ffn_up, v_w_ffn_down):
    xi, yi, ci = _place()
    chip = 2 * xi + yi
    dev = 2 * chip + ci
    s_len = x.shape[1]

    lbrow = jnp.pad(hgrn_lb_logits.reshape(1, 512), ((0, 0), (0, D - 512)))
    blk = jnp.concatenate([c, lbrow, jnp.zeros((6, D), F32)], axis=0)
    g0, = _allgather([blk], name="ag_cond", in_vmem=True)
    c16 = jnp.concatenate([g0[:, 0], c_ctx[None], jnp.zeros((7, D), F32)], axis=0)
    lg = g0[0::2, 1, :512].reshape(4, 2, 2, 128).transpose(1, 2, 0, 3).reshape(2, 2, HGW)

    nada = w_ada.shape[2]
    b_sh = lax.dynamic_slice(b_ada, (0, chip * nada), (1, nada))
    mod_sh = _ada_fwd(c16, w_ada[0], b_sh)
    g1, = _allgather([mod_sh], name="ag_mod", in_vmem=True)
    modall = g1[0::2].transpose(1, 0, 2).reshape(16, 4 * nada)
    mod = lax.dynamic_slice(modall, (dev, 0), (1, 6 * D)).reshape(6, D)
    modc = modall[8].reshape(6, D)[:2]

    shards = [w_in[0].T, w_branch_hgrn[0], w_branch_attn[0], w_out[0], w_ffn_gate[0].T, w_ffn_up[0].T,
              w_ffn_down[0]]
    bufs = _cast_place(shards, ci, dev)
    in8, = _allgather_inplace(bufs[0:1], name="ag_w_in")

    sq, gx, _, small, rs = _local_step(
        x[0], ctx[0], loss_target[0], mod, modc, norm_mix_w, norm_ffn_w, lg, hgrn_norm_w, q_norm_w,
        k_norm_w, attn_sinks[0], in8.reshape(NCOL, D), bufs[1:], dist=(ci, chip))
    loss = lax.psum(0.5 * jnp.sum(sq) / D, ("x", "y", "c"))

    def whole(r):
        return r.reshape(2 * r.shape[1], r.shape[2])

    g_dn, g_g, g_u = [whole(r) for r in rs["ffn_done"]]
    g_bh, g_ba, g_o = [whole(r) for r in rs["mix_done"]]
    in_pairs = _rs_pair_add(rs["in_units"], rs["in_recv"], ci)

    g2, tot = _ag_small(small["raw"])
    dmodc_tot = jnp.pad(tot[6:8].reshape(1, 2 * D), ((0, 0), (0, 4 * D)))
    g_b_ada = tot[0:6].reshape(1, 6 * D) + dmodc_tot
    dmod16 = jnp.concatenate([g2[:, 0:6].reshape(8, 6 * D), dmodc_tot, jnp.zeros((7, 6 * D), F32)], axis=0)
    g_w_ada, gc_part = _ada_bwd(c16, lax.dynamic_slice(dmod16, (0, chip * nada), (16, nada)), w_ada[0])
    g3, = _allgather([gc_part[8:16]], name="ag_cctx", in_vmem=True)
    g_c_ctx = _cctx_grad(g3[0::2, 0], c_ctx[None])[0]
    g_nw1 = tot[8:9]
    g_nw2 = tot[9:10]
    g_hw = tot[10, :HGW].reshape(4, HGD).sum(0, keepdims=True)
    g_qnw = tot[10, HGW:].reshape(8, HDIM).sum(0, keepdims=True)
    g_knw = tot[11, :128].reshape(2, HDIM).sum(0, keepdims=True)
    g_sinks = tot[16:24, 0][None]
    g_lg = lax.dynamic_slice(tot[12:16, :HGW].reshape(2, 2, HGW), (0, 0, chip * 128), (2, 2, 128))

    names = ["c_ctx", "w_ada", "b_ada", "norm_mix_w", "norm_ffn_w", "w_in", "hgrn_lb_logits", "hgrn_norm_w",
             "q_norm_w", "k_norm_w", "attn_sinks", "w_branch_hgrn", "w_branch_attn", "w_out", "w_ffn_gate",
             "w_ffn_up", "w_ffn_down"]
    ws = dict(zip(names, [c_ctx, w_ada, b_ada, norm_mix_w, norm_ffn_w, w_in, hgrn_lb_logits, hgrn_norm_w,
                          q_norm_w, k_norm_w, attn_sinks, w_branch_hgrn, w_branch_attn, w_out, w_ffn_gate,
                          w_ffn_up, w_ffn_down]))
    ms = dict(zip(names, [m_c_ctx, m_w_ada, m_b_ada, m_norm_mix_w, m_norm_ffn_w, m_w_in, m_hgrn_lb_logits,
                          m_hgrn_norm_w, m_q_norm_w, m_k_norm_w, m_attn_sinks, m_w_branch_hgrn,
                          m_w_branch_attn, m_w_out, m_w_ffn_gate, m_w_ffn_up, m_w_ffn_down]))
    vs = dict(zip(names, [v_c_ctx, v_w_ada, v_b_ada, v_norm_mix_w, v_norm_ffn_w, v_w_in, v_hgrn_lb_logits,
                          v_hgrn_norm_w, v_q_norm_w, v_k_norm_w, v_attn_sinks, v_w_branch_hgrn,
                          v_w_branch_attn, v_w_out, v_w_ffn_gate, v_w_ffn_up, v_w_ffn_down]))
    transposed = ("w_in", "w_ffn_gate", "w_ffn_up")

    def view(a, n):
        return a[0].T if n in transposed else a[0]

    def unview(a, n):
        return a.T[None] if n in transposed else a[None]

    delta, new_m, new_v, grads = {}, {}, {}, {}

    def big_adamw(group, gs, name, carry=None):
        (d_, m_, v_), extra = _adamw_multi([view(ws[n], n) for n in group], gs, [view(ms[n], n) for n in group],
                                           [view(vs[n], n) for n in group], name=name, carry=carry)
        for i, n in enumerate(group):
            grads[n], delta[n], new_m[n], new_v[n] = (unview(gs[i], n), unview(d_[i], n), unview(m_[i], n),
                                                      unview(v_[i], n))
        return extra

    in_contribs = big_adamw(["w_ffn_down", "w_ffn_gate", "w_ffn_up", "w_out", "w_branch_hgrn", "w_branch_attn"],
                            [g_dn, g_g, g_u, g_o, g_bh, g_ba], "adamw_first", carry=_carry_chipx(in_pairs))
    in_reds = _rs_chip_add(in_pairs, in_contribs, ci, chip)
    g_in, = [whole(r) for r in _rs_sibling_gather(in_reds)]
    big_adamw(["w_in", "w_ada"], [g_in, g_w_ada], "adamw_second")
    grads.update(c_ctx=g_c_ctx, b_ada=g_b_ada, norm_mix_w=g_nw1, norm_ffn_w=g_nw2, hgrn_lb_logits=g_lg,
                 hgrn_norm_w=g_hw, q_norm_w=g_qnw, k_norm_w=g_knw, attn_sinks=g_sinks)
    small_names = [n for n in names if n not in delta]

    def two_d(a):
        return a.reshape(1, -1) if a.ndim == 1 else a

    sd, sm_, sv = _adamw_small(*[[two_d(d[n]) for n in small_names] for d in (ws, grads, ms, vs)])
    for i, n in enumerate(small_names):
        for dst, src in ((delta, sd), (new_m, sm_), (new_v, sv)):
            dst[n] = src[i].reshape(ws[n].shape)
    return (loss, gx[None], *[grads[n] for n in names], *[delta[n] for n in names],
            *[new_m[n] for n in names], *[new_v[n] for n in names])
```

```python
import functools

import numpy as np
import jax
import jax.numpy as jnp
from jax import lax
from jax.experimental import pallas as pl
from jax.experimental.pallas import tpu as pltpu

F32 = jnp.float32
BF16 = jnp.bfloat16
HI = lax.Precision.HIGHEST
MESH = pl.DeviceIdType.MESH

D = 1024
L = 256
TM = 256
HGW = 512
HGD = 128
CH = 32
ATW = 512
HDIM = 64
BLK = 128
GRID_W = 64
DFF = 2816
NCOL = 5376
EPS = 1e-6
ROPE_THETA = 10000.0

C_FB, C_INP, C_QHG, C_FF = 0, 1, 2, 3
C_GATES = 1
C_GHG, C_QRAW = 8, 9
C_KV = 20
C_QKV = 6

ADAM_LR, ADAM_B1, ADAM_B2, ADAM_EPS, ADAM_WD, ADAM_STEP = 0.001, 0.9, 0.999, 1e-08, 0.01, 10

NN = (((1,), (0,)), ((), ()))
NT = (((1,), (1,)), ((), ()))
TN = (((0,), (0,)), ((), ()))


def _dot(a, b, dims=NN, prec=None):
    return lax.dot_general(a, b, dims, precision=prec, preferred_element_type=F32)


def _bdot(a, b, dims=NN):
    return _dot(a.astype(BF16), b.astype(BF16), dims)


def _sig(x):
    return 1.0 / (1.0 + jnp.exp(-x))


class _Carry:
    def __init__(self, ins, outs, aliases, scratch, phases):
        self.ins, self.outs, self.aliases, self.scratch, self.phases = ins, outs, aliases, scratch, phases


def _in_hbm(args):
    return [pltpu.with_memory_space_constraint(a, pltpu.HBM) for a in args]


def _carry_join(a, b):
    na_in, na_out, na_sc = len(a.ins), len(a.outs), len(a.scratch)
    aliases = dict(a.aliases)
    aliases.update({na_in + i: na_out + o for i, o in b.aliases.items()})

    def phases(ins, outs, sems):
        pa = a.phases(ins[:na_in], outs[:na_out], sems[:na_sc])
        pb = b.phases(ins[na_in:], outs[na_out:], sems[na_sc:])

        def both(fa, fb):
            if fa is None and fb is None:
                return None

            def run():
                for fn in (fa, fb):
                    if fn is not None:
                        fn()
            return run

        return tuple(both(fa, fb) for fa, fb in zip(pa, pb))

    return _Carry(list(a.ins) + list(b.ins), list(a.outs) + list(b.outs), aliases,
                  list(a.scratch) + list(b.scratch), phases)


def _pcall(body, *, name, grid, in_specs, out_specs, out_shape, scratch=(), aliases=None, vmem_mb=48,
           carry=None):
    params = pltpu.CompilerParams(dimension_semantics=("arbitrary",) * len(grid),
                                  vmem_limit_bytes=vmem_mb << 20)
    if carry is None:
        plain = pl.pallas_call(
            body, name=name, grid=grid, in_specs=in_specs, out_specs=out_specs, out_shape=out_shape,
            scratch_shapes=list(scratch), input_output_aliases=aliases or {}, compiler_params=params)
        return lambda *args: plain(*_in_hbm(args))
    single = not isinstance(out_shape, (list, tuple))
    out_specs_l = [out_specs] if single else list(out_specs)
    out_shape_l = [out_shape] if single else list(out_shape)
    n_in, n_out, n_sc = len(in_specs), len(out_shape_l), len(scratch)
    k_in, k_out = len(carry.ins), len(carry.outs)
    nsteps = int(np.prod(grid))
    assert nsteps >= 3

    def wrapped(*refs):
        ins, cins = refs[:n_in], refs[n_in:n_in + k_in]
        o0 = n_in + k_in
        outs, couts = refs[o0:o0 + n_out], refs[o0 + n_out:o0 + n_out + k_out]
        s0 = o0 + n_out + k_out
        sc, csc = refs[s0:s0 + n_sc], refs[s0 + n_sc:]
        step = pl.program_id(0)
        for ax in range(1, len(grid)):
            step = step * grid[ax] + pl.program_id(ax)
        start, mid, end = carry.phases(cins, couts, csc)
        pl.when(step == 0)(start)
        body(*ins, *outs, *sc)
        if mid is not None:
            pl.when(step == nsteps - 2)(mid)
        pl.when(step == nsteps - 1)(end)

    all_aliases = dict(aliases or {})
    all_aliases.update({n_in + i: n_out + o for i, o in carry.aliases.items()})
    call = pl.pallas_call(
        wrapped, name=name, grid=grid, in_specs=list(in_specs) + [ANY] * k_in,
        out_specs=out_specs_l + [ANY] * k_out, out_shape=out_shape_l + list(carry.outs),
        scratch_shapes=list(scratch) + list(carry.scratch), input_output_aliases=all_aliases,
        compiler_params=params)

    def run(*args):
        res = call(*_in_hbm(args), *carry.ins)
        core = res[:n_out]
        return (core[0] if single else list(core)), list(res[n_out:])

    return run


def _full(shape):
    nd = len(shape)
    return pl.BlockSpec(shape, lambda *_: (0,) * nd)


ANY = pl.BlockSpec(memory_space=pl.ANY)


def _mm(a, b, *, name, mode="nn", out_dtype=F32, tm, tn, tk):
    if mode == "nn":
        (m, k), (k2, n) = a.shape, b.shape
    elif mode == "nt":
        (m, k), (n, k2) = a.shape, b.shape
    else:
        (k, m), (k2, n) = a.shape, b.shape
    assert k == k2 and m % tm == 0 and n % tn == 0 and k % tk == 0, (name, a.shape, b.shape)
    nk = k // tk
    dims = {"nn": NN, "nt": NT, "tn": TN}[mode]

    def body(a_ref, b_ref, o_ref, acc):
        kk = pl.program_id(2)

        @pl.when(kk == 0)
        def _():
            acc[...] = jnp.zeros_like(acc)

        acc[...] += _bdot(a_ref[...], b_ref[...], dims)

        @pl.when(kk == nk - 1)
        def _():
            o_ref[...] = acc[...].astype(out_dtype)

    a_spec = (pl.BlockSpec((tk, tm), lambda i, j, kk: (kk, i)) if mode == "tn"
              else pl.BlockSpec((tm, tk), lambda i, j, kk: (i, kk)))
    b_spec = (pl.BlockSpec((tn, tk), lambda i, j, kk: (j, kk)) if mode == "nt"
              else pl.BlockSpec((tk, tn), lambda i, j, kk: (kk, j)))
    return _pcall(body, name=name, grid=(m // tm, n // tn, nk), in_specs=[a_spec, b_spec],
                  out_specs=pl.BlockSpec((tm, tn), lambda i, j, kk: (i, j)),
                  out_shape=jax.ShapeDtypeStruct((m, n), out_dtype),
                  scratch=[pltpu.VMEM((tm, tn), F32)])(a, b)


NT_IN = NCOL // 256


def _src_block(j):
    return j + jnp.where(j < 4, 2, jnp.where(j < 6, 3, jnp.where(j < 8, -6, jnp.where(
        j < 16, 5, jnp.where(j < 20, -7, -14)))))


def _mm_in(h, wt, tm, carry=None):
    tt = h.shape[0]

    def body(h_ref, w_ref, o_ref):
        o_ref[...] = _bdot(h_ref[...], w_ref[...], NT)

    return _pcall(body, name="mm_in", grid=(tt // tm, NT_IN),
                  in_specs=[pl.BlockSpec((tm, D), lambda i, j: (i, 0)),
                            pl.BlockSpec((256, D), lambda i, j: (_src_block(j), 0))],
                  out_specs=pl.BlockSpec((tm, 256), lambda i, j: (i, j)),
                  out_shape=jax.ShapeDtypeStruct((tt, NCOL), F32), carry=carry)(h, wt)


def _mm_dh(dp, wt, tm, carry=None):
    tt = dp.shape[0]

    def body(d_ref, w_ref, o_ref, acc):
        kk = pl.program_id(1)

        @pl.when(kk == 0)
        def _():
            acc[...] = jnp.zeros_like(acc)

        acc[...] += _bdot(d_ref[...], w_ref[...])

        @pl.when(kk == NT_IN - 1)
        def _():
            o_ref[...] = acc[...]

    return _pcall(body, name="mm_dh", grid=(tt // tm, NT_IN),
                  in_specs=[pl.BlockSpec((tm, 256), lambda i, kk: (i, kk)),
                            pl.BlockSpec((256, D), lambda i, kk: (_src_block(kk), 0))],
                  out_specs=pl.BlockSpec((tm, D), lambda i, kk: (i, 0)),
                  out_shape=jax.ShapeDtypeStruct((tt, D), F32), scratch=[pltpu.VMEM((tm, D), F32)],
                  carry=carry)(dp, wt)


def _mm_gin(dp, h, tk):
    tt = dp.shape[0]
    nk = tt // tk

    def body(d_ref, h_ref, o_ref, acc):
        kk = pl.program_id(1)

        @pl.when(kk == 0)
        def _():
            acc[...] = jnp.zeros_like(acc)

        acc[...] += _bdot(d_ref[...], h_ref[...], TN)

        @pl.when(kk == nk - 1)
        def _():
            o_ref[...] = acc[...]

    return _pcall(body, name="mm_gin", grid=(NT_IN, nk),
                  in_specs=[pl.BlockSpec((tk, 256), lambda j, kk: (kk, j)),
                            pl.BlockSpec((tk, D), lambda j, kk: (kk, 0))],
                  out_specs=pl.BlockSpec((256, D), lambda j, kk: (_src_block(j), 0)),
                  out_shape=jax.ShapeDtypeStruct((NCOL, D), F32), scratch=[pltpu.VMEM((256, D), F32)])(dp, h)


def _modulate(xin, nw, ss, *, name, sel):
    rows = xin.shape[0]

    def body(x_ref, nw_ref, ss_ref, h_ref):
        x = x_ref[...]
        r = lax.rsqrt(jnp.mean(x * x, axis=-1, keepdims=True) + EPS)
        s = ss_ref[0]
        h_ref[...] = ((x * r * nw_ref[...]) * (1.0 + s[1:2]) + s[0:1]).astype(BF16)

    return _pcall(body, name=name, grid=(rows // TM,),
                  in_specs=[pl.BlockSpec((TM, D), lambda i: (i, 0)), _full((1, D)),
                            pl.BlockSpec((1, 2, D), lambda i: (sel(i), 0, 0))],
                  out_specs=pl.BlockSpec((TM, D), lambda i: (i, 0)),
                  out_shape=jax.ShapeDtypeStruct((rows, D), BF16))(xin, nw, ss)


def _norm_bwd_rows(x, dh, nw, scale):
    r = lax.rsqrt(jnp.mean(x * x, axis=-1, keepdims=True) + EPS)
    xh = x * r
    dxh = dh * ((1.0 + scale) * nw)
    dx = r * (dxh - xh * jnp.mean(dxh * xh, axis=-1, keepdims=True))
    return dx, xh


def _res1_mod2(x, ao, g1, nw2, ss2):
    s_len = x.shape[0]

    def body(x_ref, ao_ref, g_ref, nw_ref, ss_ref, x1_ref, h_ref):
        x1 = x_ref[...] + g_ref[...] * ao_ref[...]
        x1_ref[...] = x1
        r = lax.rsqrt(jnp.mean(x1 * x1, axis=-1, keepdims=True) + EPS)
        s = ss_ref[0]
        h_ref[...] = ((x1 * r * nw_ref[...]) * (1.0 + s[1:2]) + s[0:1]).astype(BF16)

    row = pl.BlockSpec((TM, D), lambda i: (i, 0))
    return _pcall(body, name="res1_mod2", grid=(s_len // TM,),
                  in_specs=[row, row, _full((1, D)), _full((1, D)), _full((1, 2, D))],
                  out_specs=[row, row],
                  out_shape=[jax.ShapeDtypeStruct((s_len, D), F32),
                             jax.ShapeDtypeStruct((s_len, D), BF16)])(x, ao, g1, nw2, ss2)


TS = 1024


def _acc_call(body, *, name, grid, in_specs, out_specs, out_shape, acc_shapes, args):
    return _pcall(body, name=name, grid=grid, in_specs=in_specs, out_specs=out_specs, out_shape=out_shape,
                  scratch=[pltpu.VMEM(s, F32) for s in acc_shapes])(*args)


def _mm_cs(a, w4, *, name):
    m, k = a.shape
    _, _, ns = w4.shape

    def body(a_ref, w_ref, o_ref):
        o_ref[...] = _bdot(a_ref[...], w_ref[0])

    return _pcall(body, name=name, grid=(m // TS, 4),
                  in_specs=[pl.BlockSpec((TS, k), lambda i, j: (i, 0)),
                            pl.BlockSpec((1, k, ns), lambda i, j: (j, 0, 0))],
                  out_specs=pl.BlockSpec((TS, ns), lambda i, j: (i, j)),
                  out_shape=jax.ShapeDtypeStruct((m, 4 * ns), F32))(a, w4)


def _mm_cs_nt(a, w4, *, name):
    m = a.shape[0]
    _, k, ns = w4.shape

    def body(a_ref, w_ref, o_ref, acc):
        j = pl.program_id(1)

        @pl.when(j == 0)
        def _():
            acc[...] = jnp.zeros_like(acc)

        acc[...] += _bdot(a_ref[...], w_ref[0], NT)

        @pl.when(j == 3)
        def _():
            o_ref[...] = acc[...]

    return _acc_call(body, name=name, grid=(m // TS, 4),
                     in_specs=[pl.BlockSpec((TS, ns), lambda i, j: (i, j)),
                               pl.BlockSpec((1, k, ns), lambda i, j: (j, 0, 0))],
                     out_specs=pl.BlockSpec((TS, k), lambda i, j: (i, 0)),
                     out_shape=jax.ShapeDtypeStruct((m, k), F32), acc_shapes=[(TS, k)], args=(a, w4))


def _mm_cs_tn(a, b, ns, *, name):
    s_len, k = a.shape
    nk = s_len // TS

    def body(a_ref, b_ref, o_ref, acc):
        t = pl.program_id(1)

        @pl.when(t == 0)
        def _():
            acc[...] = jnp.zeros_like(acc)

        acc[...] += _bdot(a_ref[...], b_ref[...], TN)

        @pl.when(t == nk - 1)
        def _():
            o_ref[0] = acc[...]

    return _acc_call(body, name=name, grid=(4, nk),
                     in_specs=[pl.BlockSpec((TS, k), lambda j, t: (t, 0)),
                               pl.BlockSpec((TS, ns), lambda j, t: (t, j))],
                     out_specs=pl.BlockSpec((1, k, ns), lambda j, t: (j, 0, 0)),
                     out_shape=jax.ShapeDtypeStruct((4, k, ns), F32), acc_shapes=[(k, ns)], args=(a, b))


def _ffn_up(h2, g4, u4):
    s_len = h2.shape[0]
    ns = g4.shape[1]

    def body(h_ref, g_ref, u_ref, a_ref, b_ref, z_ref):
        h = h_ref[...]
        a = _bdot(h, g_ref[0], NT)
        b = _bdot(h, u_ref[0], NT)
        a_ref[0] = a.astype(BF16)
        b_ref[0] = b.astype(BF16)
        z_ref[0] = (a * _sig(a) * b).astype(BF16)

    w = pl.BlockSpec((1, ns, D), lambda i, j: (j, 0, 0))
    o = pl.BlockSpec((1, TS, ns), lambda i, j: (j, i, 0))
    f = jax.ShapeDtypeStruct((4, s_len, ns), BF16)
    return _pcall(body, name="ffn_up", grid=(s_len // TS, 4),
                  in_specs=[pl.BlockSpec((TS, D), lambda i, j: (i, 0)), w, w], out_specs=[o, o, o],
                  out_shape=[f, f, jax.ShapeDtypeStruct((4, s_len, ns), BF16)])(h2, g4, u4)


def _ffn_down(z4, dn4):
    _, s_len, ns = z4.shape

    def body(z_ref, w_ref, o_ref, acc):
        j = pl.program_id(1)

        @pl.when(j == 0)
        def _():
            acc[...] = jnp.zeros_like(acc)

        acc[...] += _bdot(z_ref[0], w_ref[0])

        @pl.when(j == 3)
        def _():
            o_ref[...] = acc[...]

    return _acc_call(body, name="ffn_down", grid=(s_len // TS, 4),
                     in_specs=[pl.BlockSpec((1, TS, ns), lambda i, j: (j, i, 0)),
                               pl.BlockSpec((1, ns, D), lambda i, j: (j, 0, 0))],
                     out_specs=pl.BlockSpec((TS, D), lambda i, j: (i, 0)),
                     out_shape=jax.ShapeDtypeStruct((s_len, D), F32), acc_shapes=[(TS, D)], args=(z4, dn4))


def _ffn_dz(dyb, dn4, a4, b4):
    _, s_len, ns = a4.shape

    def body(dy_ref, w_ref, a_ref, b_ref, da_ref, db_ref):
        dz = _bdot(dy_ref[...], w_ref[0], NT)
        a = a_ref[0].astype(F32)
        s = _sig(a)
        da_ref[0] = (dz * b_ref[0].astype(F32) * (s * (1.0 + a * (1.0 - s)))).astype(BF16)
        db_ref[0] = (dz * (a * s)).astype(BF16)

    t = pl.BlockSpec((1, TS, ns), lambda i, j: (j, i, 0))
    o = jax.ShapeDtypeStruct((4, s_len, ns), BF16)
    return _pcall(body, name="ffn_dz", grid=(s_len // TS, 4),
                  in_specs=[pl.BlockSpec((TS, D), lambda i, j: (i, 0)),
                            pl.BlockSpec((1, ns, D), lambda i, j: (j, 0, 0)), t, t],
                  out_specs=[t, t], out_shape=[o, o])(dyb, dn4, a4, b4)


def _ffn_gdn(z4, dyb):
    _, s_len, ns = z4.shape
    nk = s_len // TS

    def body(z_ref, dy_ref, o_ref, acc):
        t = pl.program_id(1)

        @pl.when(t == 0)
        def _():
            acc[...] = jnp.zeros_like(acc)

        acc[...] += _bdot(z_ref[0], dy_ref[...], TN)

        @pl.when(t == nk - 1)
        def _():
            o_ref[0] = acc[...]

    return _acc_call(body, name="ffn_gdn", grid=(4, nk),
                     in_specs=[pl.BlockSpec((1, TS, ns), lambda j, t: (j, t, 0)),
                               pl.BlockSpec((TS, D), lambda j, t: (t, 0))],
                     out_specs=pl.BlockSpec((1, ns, D), lambda j, t: (j, 0, 0)),
                     out_shape=jax.ShapeDtypeStruct((4, ns, D), F32), acc_shapes=[(ns, D)], args=(z4, dyb))


def _ffn_dh2(da4, db4, g4, u4):
    _, s_len, ns = da4.shape

    def body(da_ref, db_ref, g_ref, u_ref, o_ref, acc):
        j = pl.program_id(1)

        @pl.when(j == 0)
        def _():
            acc[...] = jnp.zeros_like(acc)

        acc[...] += _bdot(da_ref[0], g_ref[0]) + _bdot(db_ref[0], u_ref[0])

        @pl.when(j == 3)
        def _():
            o_ref[...] = acc[...]

    t = pl.BlockSpec((1, TS, ns), lambda i, j: (j, i, 0))
    w = pl.BlockSpec((1, ns, D), lambda i, j: (j, 0, 0))
    return _acc_call(body, name="ffn_dh2", grid=(s_len // TS, 4), in_specs=[t, t, w, w],
                     out_specs=pl.BlockSpec((TS, D), lambda i, j: (i, 0)),
                     out_shape=jax.ShapeDtypeStruct((s_len, D), F32), acc_shapes=[(TS, D)],
                     args=(da4, db4, g4, u4))


def _ffn_ggu(h2, da4, db4):
    _, s_len, ns = da4.shape
    nk = s_len // TS

    def body(h_ref, da_ref, db_ref, gg_ref, gu_ref, acc_g, acc_u):
        t = pl.program_id(1)

        @pl.when(t == 0)
        def _():
            acc_g[...] = jnp.zeros_like(acc_g)
            acc_u[...] = jnp.zeros_like(acc_u)

        h = h_ref[...]
        acc_g[...] += _bdot(da_ref[0], h, TN)
        acc_u[...] += _bdot(db_ref[0], h, TN)

        @pl.when(t == nk - 1)
        def _():
            gg_ref[0] = acc_g[...]
            gu_ref[0] = acc_u[...]

    d = pl.BlockSpec((1, TS, ns), lambda j, t: (j, t, 0))
    o = pl.BlockSpec((1, ns, D), lambda j, t: (j, 0, 0))
    f = jax.ShapeDtypeStruct((4, ns, D), F32)
    return _acc_call(body, name="ffn_ggu", grid=(4, nk),
                     in_specs=[pl.BlockSpec((TS, D), lambda j, t: (t, 0)), d, d], out_specs=[o, o],
                     out_shape=[f, f], acc_shapes=[(ns, D), (ns, D)], args=(h2, da4, db4))


def _loss_head(x1, y, g2, tgt):
    s_len = x1.shape[0]

    def body(x1_ref, y_ref, g_ref, t_ref, sq_ref, dx2_ref, dyb_ref, dg_ref):
        i = pl.program_id(0)

        @pl.when(i == 0)
        def _():
            sq_ref[...] = jnp.zeros_like(sq_ref)
            dg_ref[...] = jnp.zeros_like(dg_ref)

        y_ = y_ref[...]
        g = g_ref[...]
        e = x1_ref[...] + g * y_ - t_ref[...]
        sq_ref[...] += jnp.sum(e * e, axis=0, keepdims=True)
        dx2 = e * (1.0 / D)
        dx2_ref[...] = dx2
        dyb_ref[...] = (g * dx2).astype(BF16)
        dg_ref[...] += jnp.sum(dx2 * y_, axis=0, keepdims=True)

    row = pl.BlockSpec((TM, D), lambda i: (i, 0))
    vec = _full((1, D))
    return _pcall(body, name="loss_head", grid=(s_len // TM,),
                  in_specs=[row, row, vec, row], out_specs=[vec, row, row, vec],
                  out_shape=[jax.ShapeDtypeStruct((1, D), F32), jax.ShapeDtypeStruct((s_len, D), F32),
                             jax.ShapeDtypeStruct((s_len, D), BF16),
                             jax.ShapeDtypeStruct((1, D), F32)])(x1, y, g2, tgt)


def _mod2_bwd(x1, dh2, dx2, ao, nw2, ss2, g1):
    s_len = x1.shape[0]

    def body(x1_ref, dh_ref, dx2_ref, ao_ref, nw_ref, ss_ref, g_ref,
             dx1_ref, da_ref, dss_ref, dnw_ref, dg_ref):
        i = pl.program_id(0)

        @pl.when(i == 0)
        def _():
            dss_ref[...] = jnp.zeros_like(dss_ref)
            dnw_ref[...] = jnp.zeros_like(dnw_ref)
            dg_ref[...] = jnp.zeros_like(dg_ref)

        dh = dh_ref[...]
        nw = nw_ref[...]
        scale = ss_ref[0][1:2]
        dxn, xh = _norm_bwd_rows(x1_ref[...], dh, nw, scale)
        dx1 = dx2_ref[...] + dxn
        dx1_ref[...] = dx1
        da_ref[...] = (g_ref[...] * dx1).astype(BF16)
        dg_ref[...] += jnp.sum(dx1 * ao_ref[...], axis=0, keepdims=True)
        dsh = jnp.sum(dh, axis=0, keepdims=True)
        dsc = jnp.sum(dh * xh * nw, axis=0, keepdims=True)
        dss_ref[...] += jnp.concatenate([dsh, dsc], axis=0)
        dnw_ref[...] += jnp.sum(dh * xh * (1.0 + scale), axis=0, keepdims=True)

    row = pl.BlockSpec((TM, D), lambda i: (i, 0))
    vec = _full((1, D))
    return _pcall(body, name="mod2_bwd", grid=(s_len // TM,),
                  in_specs=[row, row, row, row, vec, _full((1, 2, D)), vec],
                  out_specs=[row, row, _full((2, D)), vec, vec],
                  out_shape=[jax.ShapeDtypeStruct((s_len, D), F32), jax.ShapeDtypeStruct((s_len, D), BF16),
                             jax.ShapeDtypeStruct((2, D), F32), jax.ShapeDtypeStruct((1, D), F32),
                             jax.ShapeDtypeStruct((1, D), F32)])(x1, dh2, dx2, ao, nw2, ss2, g1)


def _mod1_bwd(tok, dh, dx1, nw1, ss1, carry=None):
    tt = tok.shape[0]
    s_len = dx1.shape[0]

    def body(t_ref, dh_ref, dx1_ref, nw_ref, ss_ref, dx_ref, dss_ref, dnw_ref):
        i = pl.program_id(0)

        @pl.when(i == 0)
        def _():
            dnw_ref[...] = jnp.zeros_like(dnw_ref)

        @pl.when(i <= 1)
        def _():
            dss_ref[...] = jnp.zeros_like(dss_ref)

        dh_ = dh_ref[...]
        nw = nw_ref[...]
        scale = ss_ref[0][1:2]
        dxn, xh = _norm_bwd_rows(t_ref[...], dh_, nw, scale)

        @pl.when(i >= 1)
        def _():
            dx_ref[...] = dx1_ref[...] + dxn

        dsh = jnp.sum(dh_, axis=0, keepdims=True)
        dsc = jnp.sum(dh_ * xh * nw, axis=0, keepdims=True)
        dss_ref[...] += jnp.concatenate([dsh, dsc], axis=0)[None]
        dnw_ref[...] += jnp.sum(dh_ * xh * (1.0 + scale), axis=0, keepdims=True)

    row = pl.BlockSpec((TM, D), lambda i: (i, 0))
    lat = pl.BlockSpec((TM, D), lambda i: (jnp.maximum(i - 1, 0), 0))
    sel = pl.BlockSpec((1, 2, D), lambda i: (jnp.minimum(i, 1), 0, 0))
    return _pcall(body, name="mod1_bwd", grid=(tt // TM,),
                  in_specs=[row, row, lat, _full((1, D)), sel],
                  out_specs=[lat, sel, _full((1, D))],
                  out_shape=[jax.ShapeDtypeStruct((s_len, D), F32), jax.ShapeDtypeStruct((2, 2, D), F32),
                             jax.ShapeDtypeStruct((1, D), F32)], carry=carry)(tok, dh, dx1, nw1, ss1)


def _rows(c):
    return slice(c * CH, (c + 1) * CH)


def _chunk_masks(rev, transpose=False):
    r = lax.broadcasted_iota(jnp.int32, (TM, TM), 0)
    c = lax.broadcasted_iota(jnp.int32, (TM, TM), 1)
    same = (r // CH) == (c // CH)
    before = (c >= r) if (rev != transpose) else (c <= r)
    return same & before, same


def _hgrn_gate(fl, qraw, lg):
    lb = 1.0 / (1.0 + jnp.exp(lg[1:2] - lg[0:1]))
    sg = _sig(fl)
    f = lb + (1.0 - lb) * sg
    q = qraw * _sig(qraw) * (HGD ** -0.5)
    return lb, sg, f, q


def _hgrn_fwd(p, lg, *, rev, carry=None):
    tt = p.shape[0]
    nt = tt // TM
    ncht = TM // CH
    d = 1 if rev else 0

    def tile_of(s):
        return jnp.where(s == 0, 0, nt - s) if rev else s

    def body(f_ref, inp_ref, q_ref, lg_ref, o_ref, st_ref, state):
        s = pl.program_id(0)

        @pl.when(s == 0)
        def _():
            state[...] = jnp.zeros_like(state)

        _, _, f, q = _hgrn_gate(f_ref[...], q_ref[...], lg_ref[0])
        lf = jnp.log(f)
        causal, same = _chunk_masks(rev)
        cum = _dot(causal.astype(F32), lf, prec=HI)
        tot = _dot(same.astype(F32), lf, prec=HI)
        qd = (q * jnp.exp(cum)).astype(BF16)
        kd = ((1.0 - f) * jnp.exp(-cum)).astype(BF16)
        ke = ((1.0 - f) * jnp.exp(tot - cum)).astype(BF16)
        et = jnp.exp(tot)
        v = inp_ref[...].astype(BF16)
        order = range(ncht - 1, -1, -1) if rev else range(ncht)
        outs = []
        for h in range(4):
            sl = slice(h * HGD, (h + 1) * HGD)
            qd_, kd_, ke_, v_ = qd[:, sl], kd[:, sl], ke[:, sl], v[:, sl]
            pm = jnp.where(causal, _dot(qd_, kd_, NT), 0.0).astype(BF16)
            o_h = _dot(pm, v_)
            upd = [_dot(v_[_rows(c)], ke_[_rows(c)], TN) for c in range(ncht)]
            st = state[h]
            for c in order:
                st_ref[c, h] = st
                st = st * et[c * CH:c * CH + 1, sl] + upd[c]
            state[h] = st
            inter = [_dot(qd_[_rows(c)], st_ref[c, h].astype(BF16), NT) for c in range(ncht)]
            outs.append(o_h + jnp.concatenate(inter, axis=0))
        o_ref[...] = jnp.concatenate(outs, axis=1)

    def col(cb):
        return pl.BlockSpec((TM, HGW), lambda s: (tile_of(s), cb))

    return _pcall(
        body, name="hgrn_fwd_rev" if rev else "hgrn_fwd", grid=(nt,),
        in_specs=[col(C_FB if rev else C_FF), col(C_INP), col(C_QHG),
                  pl.BlockSpec((1, 2, HGW), lambda s: (d, 0, 0))],
        out_specs=[pl.BlockSpec((TM, HGW), lambda s: (tile_of(s), 0)),
                   pl.BlockSpec((ncht, 4, HGD, HGD), lambda s: (tile_of(s), 0, 0, 0))],
        out_shape=[jax.ShapeDtypeStruct((tt, HGW), F32),
                   jax.ShapeDtypeStruct((nt * ncht, 4, HGD, HGD), F32)],
        scratch=[pltpu.VMEM((4, HGD, HGD), F32)], carry=carry)(p, p, p, lg)


def _hgrn_bwd(p, lg, do, st, dp, prev, *, rev, carry=None):
    tt = p.shape[0]
    nt = tt // TM
    ncht = TM // CH
    d = 1 if rev else 0
    second = prev is not None

    def tile_of(s):
        return jnp.where(s == nt - 1, 0, s + 1) if rev else nt - 1 - s

    def body(*refs):
        if second:
            (f_ref, inp_ref, q_ref, lg_ref, do_ref, st_ref, dvp_ref, dqp_ref, _dp_in,
             dp_ref, dlg_ref, dstate) = refs
        else:
            (f_ref, inp_ref, q_ref, lg_ref, do_ref, st_ref, _dp_in,
             dp_ref, dv_ref, dq_ref, dlg_ref, dstate) = refs
        s = pl.program_id(0)
        tile = tile_of(s)

        @pl.when(s == 0)
        def _():
            dstate[...] = jnp.zeros_like(dstate)
            dlg_ref[...] = jnp.zeros_like(dlg_ref)

        qraw = q_ref[...]
        lb, sg, f, q = _hgrn_gate(f_ref[...], qraw, lg_ref[0])
        lf = jnp.log(f)
        causal, same = _chunk_masks(rev)
        causal_t, _ = _chunk_masks(rev, transpose=True)
        cum = _dot(causal.astype(F32), lf, prec=HI)
        tot = _dot(same.astype(F32), lf, prec=HI)
        ea, eb, ee, et = jnp.exp(cum), jnp.exp(-cum), jnp.exp(tot - cum), jnp.exp(tot)
        qdf, kdf, kef = q * ea, (1.0 - f) * eb, (1.0 - f) * ee
        qd, kd, ke = qdf.astype(BF16), kdf.astype(BF16), kef.astype(BF16)
        v = inp_ref[...].astype(BF16)
        dob = jnp.where(tile == 0, 0.0, do_ref[...]).astype(BF16)
        order = range(ncht) if rev else range(ncht - 1, -1, -1)
        dq_l, dk_l, dv_l, dcum_l, dtot_l = [], [], [], [], []
        for h in range(4):
            sl = slice(h * HGD, (h + 1) * HGD)
            qd_, kd_, ke_, v_, do_ = qd[:, sl], kd[:, sl], ke[:, sl], v[:, sl], dob[:, sl]
            pmt = jnp.where(causal_t, _dot(kd_, qd_, NT), 0.0).astype(BF16)
            dpm = jnp.where(causal, _dot(do_, v_, NT), 0.0).astype(BF16)
            dpmt = jnp.where(causal_t, _dot(v_, do_, NT), 0.0).astype(BF16)
            dv = _dot(pmt, do_)
            dqd = _dot(dpm, kd_)
            dkd = _dot(dpmt, qd_)
            upd = [_dot(do_[_rows(c)], qd_[_rows(c)], TN) for c in range(ncht)]
            ds = dstate[h]
            ds1 = [None] * ncht
            for c in order:
                ds1[c] = ds
                ds = ds * et[c * CH:c * CH + 1, sl] + upd[c]
            dstate[h] = ds
            dke_c, dv_c, dqd_c, dtot_c = [], [], [], []
            for c in range(ncht):
                st0 = st_ref[c, h]
                dsb = ds1[c].astype(BF16)
                dke_ = _dot(v_[_rows(c)], dsb)
                dke_c.append(dke_)
                dv_c.append(_dot(ke_[_rows(c)], dsb, NT))
                dqd_c.append(_dot(do_[_rows(c)], st0.astype(BF16)))
                dt = (jnp.sum(ds1[c] * st0, axis=0, keepdims=True) * et[c * CH:c * CH + 1, sl]
                      + jnp.sum(dke_ * kef[_rows(c), sl], axis=0, keepdims=True))
                dtot_c.append(jnp.broadcast_to(dt, (CH, HGD)))
            dke = jnp.concatenate(dke_c, axis=0)
            dqd = dqd + jnp.concatenate(dqd_c, axis=0)
            dv_l.append(dv + jnp.concatenate(dv_c, axis=0))
            dtot_l.append(jnp.concatenate(dtot_c, axis=0))
            dq_l.append(dqd * ea[:, sl])
            dk_l.append(dkd * eb[:, sl] + dke * ee[:, sl])
            dcum_l.append(dqd * qdf[:, sl] - dkd * kdf[:, sl] - dke * kef[:, sl])
        dcum = jnp.concatenate(dcum_l, axis=1)
        dlf = _dot(causal_t.astype(F32), dcum, prec=HI) + jnp.concatenate(dtot_l, axis=1)
        dq_t = jnp.concatenate(dq_l, axis=1)
        dv_t = jnp.concatenate(dv_l, axis=1)

        df = dlf / f - jnp.concatenate(dk_l, axis=1)
        dfl = df * (1.0 - lb) * sg * (1.0 - sg)
        dlb = jnp.sum(df * (1.0 - sg), axis=0, keepdims=True)
        dl0 = dlb * lb * (1.0 - lb)
        dlg_ref[...] += jnp.concatenate([dl0, -dl0], axis=0)[None]
        if second:
            sq = _sig(qraw)
            dqr = (dqp_ref[...] + dq_t) * (HGD ** -0.5) * (sq * (1.0 + qraw * (1.0 - sq)))
            dp_ref[...] = jnp.concatenate([dfl, dvp_ref[...] + dv_t, dqr], axis=1).astype(BF16)
        else:
            dp_ref[...] = dfl.astype(BF16)
            dv_ref[...] = dv_t
            dq_ref[...] = dq_t

    def col(cb):
        return pl.BlockSpec((TM, HGW), lambda s: (tile_of(s), cb))

    tok = pl.BlockSpec((TM, HGW), lambda s: (tile_of(s), 0))
    in_specs = [col(C_FB if rev else C_FF), col(C_INP), col(C_QHG),
                pl.BlockSpec((1, 2, HGW), lambda s: (d, 0, 0)),
                pl.BlockSpec((TM, HGW), lambda s: (jnp.maximum(tile_of(s) - 1, 0), 0)),
                pl.BlockSpec((ncht, 4, HGD, HGD), lambda s: (tile_of(s), 0, 0, 0))]
    args = [p, p, p, lg, do, st]
    dlg_spec = _full((1, 2, HGW))
    dlg_shape = jax.ShapeDtypeStruct((1, 2, HGW), F32)
    if second:
        in_specs += [tok, tok]
        args += [prev[0], prev[1]]
        out_specs = [pl.BlockSpec((TM, 3 * HGW), lambda s: (tile_of(s), 0)), dlg_spec]
        out_shape = [jax.ShapeDtypeStruct(dp.shape, BF16), dlg_shape]
    else:
        out_specs = [pl.BlockSpec((TM, HGW), lambda s: (tile_of(s), C_FB if rev else C_FF)), tok, tok, dlg_spec]
        out_shape = [jax.ShapeDtypeStruct(dp.shape, BF16), jax.ShapeDtypeStruct((tt, HGW), F32),
                     jax.ShapeDtypeStruct((tt, HGW), F32), dlg_shape]
    in_specs.append(ANY)
    args.append(dp)
    return _pcall(body, name="hgrn_bwd_rev" if rev else "hgrn_bwd", grid=(nt,),
                  in_specs=in_specs, out_specs=out_specs, out_shape=out_shape,
                  scratch=[pltpu.VMEM((4, HGD, HGD), F32)],
                  aliases={len(args) - 1: 0}, carry=carry)(*args)


def _head_rms(o, w, nheads):
    outs = []
    for h in range(nheads):
        oh = o[:, h * HGD:(h + 1) * HGD]
        outs.append(oh * lax.rsqrt(jnp.mean(oh * oh, axis=-1, keepdims=True) + EPS))
    return jnp.concatenate(outs, axis=1)


def _readout(o0, o1, p, hw4):
    s_len = o0.shape[0] - L

    def body(o0_ref, o1_ref, g_ref, w_ref, y_ref):
        xh = _head_rms(o0_ref[...] + o1_ref[...], None, 4)
        g = g_ref[...]
        y_ref[...] = (xh * w_ref[...] * (g * _sig(g))).astype(BF16)

    lat = pl.BlockSpec((TM, HGW), lambda i: (i + 1, 0))
    return _pcall(body, name="readout", grid=(s_len // TM,),
                  in_specs=[lat, lat, pl.BlockSpec((TM, HGW), lambda i: (i + 1, C_GHG)), _full((1, HGW))],
                  out_specs=pl.BlockSpec((TM, HGW), lambda i: (i, 0)),
                  out_shape=jax.ShapeDtypeStruct((s_len, HGW), BF16))(o0, o1, p, hw4)


def _readout_bwd(o0, o1, p, hw4, dy, dp, carry=None):
    tt = o0.shape[0]
    s_len = tt - L

    def body(o0_ref, o1_ref, g_ref, w_ref, dy_ref, _dp_in, dp_ref, do_ref, dw_ref):
        i = pl.program_id(0)

        @pl.when(i == 0)
        def _():
            dw_ref[...] = jnp.zeros_like(dw_ref)
            dp_ref[...] = jnp.zeros_like(dp_ref)

        @pl.when(i >= 1)
        def _():
            o = o0_ref[...] + o1_ref[...]
            g = g_ref[...]
            w = w_ref[...]
            sg = _sig(g)
            dy_ = dy_ref[...]
            dsw = dy_ * (g * sg)
            outs, xhs = [], []
            for h in range(4):
                sl = slice(h * HGD, (h + 1) * HGD)
                oh = o[:, sl]
                r = lax.rsqrt(jnp.mean(oh * oh, axis=-1, keepdims=True) + EPS)
                xh = oh * r
                dxh = dsw[:, sl] * w[:, sl]
                outs.append(r * (dxh - xh * jnp.mean(dxh * xh, axis=-1, keepdims=True)))
                xhs.append(xh)
            xh = jnp.concatenate(xhs, axis=1)
            do_ref[...] = jnp.concatenate(outs, axis=1)
            dp_ref[...] = (dy_ * xh * w * (sg * (1.0 + g * (1.0 - sg)))).astype(BF16)
            dw_ref[...] += jnp.sum(dsw * xh, axis=0, keepdims=True)

    tok = pl.BlockSpec((TM, HGW), lambda i: (i, 0))
    lat = pl.BlockSpec((TM, HGW), lambda i: (jnp.maximum(i - 1, 0), 0))
    return _pcall(body, name="readout_bwd", grid=(tt // TM,),
                  in_specs=[tok, tok, pl.BlockSpec((TM, HGW), lambda i: (i, C_GHG)), _full((1, HGW)), lat, ANY],
                  out_specs=[pl.BlockSpec((TM, HGW), lambda i: (i, C_GHG)), lat, _full((1, HGW))],
                  out_shape=[jax.ShapeDtypeStruct(dp.shape, BF16), jax.ShapeDtypeStruct((s_len, HGW), F32),
                             jax.ShapeDtypeStruct((1, HGW), F32)],
                  aliases={5: 0}, carry=carry)(o0, o1, p, hw4, dy, dp)


def _rope_tables(s_len):
    t = np.arange(s_len)
    inv = ROPE_THETA ** (-np.arange(0, 32, 2, dtype=np.float64) / 32)
    def half(pos):
        ang = pos[:, None].astype(np.float64) * inv[None, :]
        return (np.concatenate([np.cos(ang), np.cos(ang)], 1), np.concatenate([-np.sin(ang), np.sin(ang)], 1))
    cr, sr = half(t // GRID_W)
    cc, sc = half(t % GRID_W)
    cos = np.concatenate([cr, cc, cr, cc], 1)
    sin = np.concatenate([sr, sc, sr, sc], 1)
    cos = np.concatenate([np.ones((L, 128)), cos], 0)
    sin = np.concatenate([np.zeros((L, 128)), sin], 0)
    return jnp.asarray(cos, F32), jnp.asarray(sin, F32)


def _blockdiag(n, w):
    i = np.arange(n)
    return jnp.asarray((i[:, None] // w == i[None, :] // w) / float(w), F32)


def _dup_matrix():
    m = np.zeros((128, 512), np.float32)
    for g in range(2):
        for j in range(4):
            for dd in range(HDIM):
                m[64 * g + dd, 256 * g + 64 * j + dd] = 1.0
    return m


def _head_mean(x, blockdiag):
    return _dot(x, blockdiag, prec=lax.Precision.HIGH)


def _rot(x):
    n = x.shape[1]
    lane = lax.broadcasted_iota(jnp.int32, x.shape, 1)
    return jnp.where((lane % 32) < 16, pltpu.roll(x, n - 16, 1), pltpu.roll(x, 16, 1))


def _qk_prep(p, cos, sin, qnw8, knw2, bd512, bd128, dup):
    tt = p.shape[0]

    def body(q_ref, kv_ref, cos_ref, sin_ref, qw_ref, kw_ref, b5_ref, b1_ref, dup_ref,
             qr_ref, k4_ref, v4_ref):
        cos_, sin_ = cos_ref[...], sin_ref[...]
        q = q_ref[...]
        qn = q * lax.rsqrt(_head_mean(q * q, b5_ref[...]) + EPS) * qw_ref[...]
        cos4 = jnp.concatenate([cos_] * 4, axis=1)
        sin4 = jnp.concatenate([sin_] * 4, axis=1)
        qr_ref[...] = ((qn * cos4 + _rot(qn) * sin4) * (HDIM ** -0.5)).astype(BF16)
        kv = kv_ref[...]
        k, v = kv[:, :128], kv[:, 128:]
        kn = k * lax.rsqrt(_head_mean(k * k, b1_ref[...]) + EPS) * kw_ref[...]
        kr = kn * cos_ + _rot(kn) * sin_
        k4_ref[...] = _bdot(kr, dup_ref[...]).astype(BF16)
        v4_ref[...] = _bdot(v, dup_ref[...]).astype(BF16)

    row = lambda w, cb: pl.BlockSpec((TM, w), lambda i: (i, cb))
    out = jax.ShapeDtypeStruct((tt, ATW), BF16)
    return _pcall(body, name="qk_prep", grid=(tt // TM,),
                  in_specs=[row(ATW, C_QRAW), row(256, C_KV), row(128, 0), row(128, 0),
                            _full((1, ATW)), _full((1, 128)), _full((ATW, ATW)), _full((128, 128)),
                            _full((128, ATW))],
                  out_specs=[row(ATW, 0)] * 3, out_shape=[out] * 3)(
                      p, p, cos, sin, qnw8, knw2, bd512, bd128, dup)


def _attn_masks(i, nb):
    r = lax.broadcasted_iota(jnp.int32, (4 * BLK, 3 * BLK + L), 0) % BLK
    c = lax.broadcasted_iota(jnp.int32, (4 * BLK, 3 * BLK + L), 1)
    kpos = (i - 1) * BLK + c
    loc = (jnp.abs(c - BLK - r) <= BLK) & (kpos >= 0) & (kpos < nb * BLK)
    return loc | (c >= 3 * BLK)


def _stack_mask():
    r = lax.broadcasted_iota(jnp.int32, (4 * BLK, 256), 0)
    lane = lax.broadcasted_iota(jnp.int32, (4 * BLK, 256), 1)
    return (r // BLK) == (lane // HDIM)


def _stack_heads(xg, fill=0.0):
    x4 = jnp.concatenate([xg] * 4, axis=0)
    return jnp.where(_stack_mask(), x4, jnp.full_like(x4, fill))


def _unstack_heads(x4):
    out = jnp.where(_lane_mask(0), x4[0:BLK], 0.0)
    for j in range(1, 4):
        out = out + jnp.where(_lane_mask(j), x4[j * BLK:(j + 1) * BLK], 0.0)
    return out


def _per_head_rows(vals):
    return jnp.concatenate([jnp.broadcast_to(v, (BLK, 1)) for v in vals], axis=0)


def _lane_mask(j):
    lane = lax.broadcasted_iota(jnp.int32, (1, 256), 1)
    return (lane // HDIM) == j


def _attn_specs(nb):
    blk = lambda off: pl.BlockSpec((BLK, ATW), lambda i: (jnp.clip(i + off, 0, nb - 1) + 2, 0))
    ctx = pl.BlockSpec((L, ATW), lambda i: (0, 0))
    return blk, ctx


def _attn_fwd(qr, k4, v4, sinks, carry=None):
    tt = qr.shape[0]
    s_len = tt - L
    nb = s_len // BLK

    def body(sk_ref, q_ref, kp, ko, kn, kc, vp, vo, vn, vc, y_ref, lse_ref):
        i = pl.program_id(0)
        valid = _attn_masks(i, nb)
        q = q_ref[...]
        ys, lses = [], []
        for g in range(2):
            gs = slice(256 * g, 256 * g + 256)
            kcat = jnp.concatenate([kp[:, gs], ko[:, gs], kn[:, gs], kc[:, gs]], axis=0)
            vcat = jnp.concatenate([vp[:, gs], vo[:, gs], vn[:, gs], vc[:, gs]], axis=0)
            sink = _per_head_rows([sk_ref[4 * g + j] for j in range(4)])
            s = jnp.where(valid, _dot(_stack_heads(q[:, gs]), kcat, NT), -1e30)
            m = jnp.maximum(jnp.max(s, axis=-1, keepdims=True), sink)
            e = jnp.exp(s - m)
            den = jnp.sum(e, axis=-1, keepdims=True) + jnp.exp(sink - m)
            ys.append(_unstack_heads(_bdot(e / den, vcat)))
            lses.append(_unstack_heads(jnp.broadcast_to(m + jnp.log(den), (4 * BLK, 256))))
        y_ref[...] = jnp.concatenate(ys, axis=1).astype(BF16)
        lse_ref[...] = jnp.concatenate(lses, axis=1)

    blk, ctx = _attn_specs(nb)
    out = pl.BlockSpec((BLK, ATW), lambda i: (i, 0))
    return _pcall(body, name="attn_fwd", grid=(nb,),
                  in_specs=[pl.BlockSpec(memory_space=pltpu.SMEM), blk(0),
                            blk(-1), blk(0), blk(1), ctx, blk(-1), blk(0), blk(1), ctx],
                  out_specs=[out, out],
                  out_shape=[jax.ShapeDtypeStruct((s_len, ATW), BF16),
                             jax.ShapeDtypeStruct((s_len, ATW), F32)], carry=carry)(
                      sinks, qr, k4, k4, k4, k4, v4, v4, v4, v4)


def _attn_bwd(qr, k4, v4, sinks, y, lse, dy, carry=None):
    tt = qr.shape[0]
    s_len = tt - L
    nb = s_len // BLK

    def body(sk_ref, q_ref, kp, ko, kn, kc, vp, vo, vn, vc, y_ref, lse_ref, dy_ref,
             dq_ref, dkw_ref, dvw_ref, dkc_ref, dvc_ref, dsk_ref):
        i = pl.program_id(0)

        @pl.when(i == 0)
        def _():
            dkc_ref[...] = jnp.zeros_like(dkc_ref)
            dvc_ref[...] = jnp.zeros_like(dvc_ref)
            dsk_ref[...] = jnp.zeros_like(dsk_ref)

        valid = _attn_masks(i, nb)
        q = q_ref[...]
        dy_ = dy_ref[...]
        dly = dy_ * y_ref[...].astype(F32)
        lse_ = lse_ref[...]
        dqs = []
        for g in range(2):
            gs = slice(256 * g, 256 * g + 256)
            kcat = jnp.concatenate([kp[:, gs], ko[:, gs], kn[:, gs], kc[:, gs]], axis=0)
            vcat = jnp.concatenate([vp[:, gs], vo[:, gs], vn[:, gs], vc[:, gs]], axis=0)
            q4 = _stack_heads(q[:, gs])
            dy4 = _stack_heads(dy_[:, gs]).astype(BF16)
            lse4 = jnp.max(_stack_heads(lse_[:, gs], fill=-1e30), axis=-1, keepdims=True)
            delta = jnp.sum(_stack_heads(dly[:, gs]), axis=-1, keepdims=True)
            sink = _per_head_rows([sk_ref[4 * g + j] for j in range(4)])
            pr = jnp.where(valid, jnp.exp(_dot(q4, kcat, NT) - lse4), 0.0)
            dsb = (pr * (_dot(dy4, vcat, NT) - delta)).astype(BF16)
            dsink = jnp.exp(sink - lse4) * delta
            for j in range(4):
                dsk_ref[4 * g + j:4 * g + j + 1, :] += jnp.broadcast_to(
                    -jnp.sum(dsink[j * BLK:(j + 1) * BLK], axis=0, keepdims=True), (1, 128))
            dqs.append(_unstack_heads(_dot(dsb, kcat)))
            dkg = _dot(dsb, q4, TN)
            dvg = _dot(pr.astype(BF16), dy4, TN)
            dkw_ref[0, :, gs] = dkg[:3 * BLK]
            dvw_ref[0, :, gs] = dvg[:3 * BLK]
            dkc_ref[:, gs] += dkg[3 * BLK:]
            dvc_ref[:, gs] += dvg[3 * BLK:]
        dq_ref[...] = jnp.concatenate(dqs, axis=1)

    blk, ctx = _attn_specs(nb)
    out = pl.BlockSpec((BLK, ATW), lambda i: (i, 0))
    win = pl.BlockSpec((1, 3 * BLK, ATW), lambda i: (i, 0, 0))
    acc = _full((L, ATW))
    return _pcall(body, name="attn_bwd", grid=(nb,),
                  in_specs=[pl.BlockSpec(memory_space=pltpu.SMEM), blk(0),
                            blk(-1), blk(0), blk(1), ctx, blk(-1), blk(0), blk(1), ctx, out, out, out],
                  out_specs=[out, win, win, acc, acc, _full((8, 128))],
                  out_shape=[jax.ShapeDtypeStruct((s_len, ATW), F32),
                             jax.ShapeDtypeStruct((nb, 3 * BLK, ATW), F32),
                             jax.ShapeDtypeStruct((nb, 3 * BLK, ATW), F32),
                             jax.ShapeDtypeStruct((L, ATW), F32), jax.ShapeDtypeStruct((L, ATW), F32),
                             jax.ShapeDtypeStruct((8, 128), F32)], carry=carry)(
                      sinks, qr, k4, k4, k4, k4, v4, v4, v4, v4, y, lse, dy)


def _attn_post(p, cos, sin, qnw8, knw2, bd512, bd128, dupt, dq, dkw, dvw, dkc, dvc, dp, carry=None):
    tt = p.shape[0]
    s_len = tt - L
    nb = s_len // BLK
    nctx = L // BLK

    def body(q_ref, kv_ref, cos_ref, sin_ref, qw_ref, kw_ref, b5_ref, b1_ref, dupt_ref,
             dq_ref, kwp, kwo, kwn, vwp, vwo, vwn, dkc_ref, dvc_ref, _dp_in,
             dp_ref, dqw_ref, dkw_ref):
        t = pl.program_id(0)
        j = t - nctx

        @pl.when(t == 0)
        def _():
            dqw_ref[...] = jnp.zeros_like(dqw_ref)
            dkw_ref[...] = jnp.zeros_like(dkw_ref)

        is_lat = t >= nctx
        cos_, sin_ = cos_ref[...], sin_ref[...]
        has_p = is_lat & (j >= 1)
        has_n = is_lat & (j <= nb - 2)
        dk4 = (jnp.where(is_lat, kwo[0], dkc_ref[...]) + jnp.where(has_p, kwp[0], 0.0)
               + jnp.where(has_n, kwn[0], 0.0))
        dv4 = (jnp.where(is_lat, vwo[0], dvc_ref[...]) + jnp.where(has_p, vwp[0], 0.0)
               + jnp.where(has_n, vwn[0], 0.0))
        dkr = _dot(dk4, dupt_ref[...], prec=HI)
        dv = _dot(dv4, dupt_ref[...], prec=HI)
        kv = kv_ref[...]
        k = kv[:, :128]
        kw = kw_ref[...]
        rk = lax.rsqrt(_head_mean(k * k, b1_ref[...]) + EPS)
        xk = k * rk
        dkn = dkr * cos_ + _rot(dkr * sin_)
        dxk = dkn * kw
        dk = rk * (dxk - xk * _head_mean(dxk * xk, b1_ref[...]))
        dkw_ref[...] += jnp.sum(dkn * xk, axis=0, keepdims=True)
        q = q_ref[...]
        qw = qw_ref[...]
        rq = lax.rsqrt(_head_mean(q * q, b5_ref[...]) + EPS)
        xq = q * rq
        cos4 = jnp.concatenate([cos_] * 4, axis=1)
        sin4 = jnp.concatenate([sin_] * 4, axis=1)
        dqr = jnp.where(is_lat, dq_ref[...], 0.0) * (HDIM ** -0.5)
        dqn = dqr * cos4 + _rot(dqr * sin4)
        dxq = dqn * qw
        dqraw = rq * (dxq - xq * _head_mean(dxq * xq, b5_ref[...]))
        dqw_ref[...] += jnp.sum(dqn * xq, axis=0, keepdims=True)
        dp_ref[...] = jnp.concatenate([dqraw, dk, dv], axis=1).astype(BF16)

    row = lambda w, cb: pl.BlockSpec((BLK, w), lambda t: (t, cb))
    lat = pl.BlockSpec((BLK, ATW), lambda t: (jnp.maximum(t - nctx, 0), 0))

    def part(off):
        return pl.BlockSpec((1, BLK, ATW), lambda t: (jnp.clip(t - nctx + off, 0, nb - 1), 1 - off, 0))

    cacc = pl.BlockSpec((BLK, ATW), lambda t: (jnp.minimum(t, nctx - 1), 0))
    return _pcall(body, name="attn_post", grid=(tt // BLK,),
                  in_specs=[row(ATW, C_QRAW), row(256, C_KV), row(128, 0), row(128, 0),
                            _full((1, ATW)), _full((1, 128)), _full((ATW, ATW)), _full((128, 128)),
                            _full((ATW, 128)), lat, part(-1), part(0), part(1), part(-1), part(0), part(1),
                            cacc, cacc, ANY],
                  out_specs=[pl.BlockSpec((BLK, 768), lambda t: (t, C_QKV)), _full((1, ATW)), _full((1, 128))],
                  out_shape=[jax.ShapeDtypeStruct(dp.shape, BF16), jax.ShapeDtypeStruct((1, ATW), F32),
                             jax.ShapeDtypeStruct((1, 128), F32)],
                  aliases={18: 0}, carry=carry)(p, p, cos, sin, qnw8, knw2, bd512, bd128, dupt,
                                   dq, dkw, dkw, dkw, dvw, dvw, dvw, dkc, dvc, dp)


def _merge(ah, aa, p):
    s_len = ah.shape[0]

    def body(ah_ref, aa_ref, gh_ref, ga_ref, m_ref):
        m_ref[...] = (_sig(gh_ref[...]) * ah_ref[...] + _sig(ga_ref[...]) * aa_ref[...]).astype(BF16)

    row = pl.BlockSpec((TM, D), lambda i: (i, 0))
    return _pcall(body, name="merge", grid=(s_len // TM,),
                  in_specs=[row, row, pl.BlockSpec((TM, D), lambda i: (i + 1, 2)),
                            pl.BlockSpec((TM, D), lambda i: (i + 1, 3))],
                  out_specs=row, out_shape=jax.ShapeDtypeStruct((s_len, D), BF16))(ah, aa, p, p)


def _merge_bwd(dm, ah, aa, p, carry=None):
    tt = p.shape[0]
    s_len = tt - L

    def body(dm_ref, ah_ref, aa_ref, gh_ref, ga_ref, dp_ref, dmh_ref, dma_ref):
        i = pl.program_id(0)

        @pl.when(i == 0)
        def _():
            dp_ref[...] = jnp.zeros_like(dp_ref)

        @pl.when(i >= 1)
        def _():
            dm_ = dm_ref[...]
            sh, sa = _sig(gh_ref[...]), _sig(ga_ref[...])
            dp_ref[...] = jnp.concatenate([dm_ * ah_ref[...] * sh * (1.0 - sh),
                                           dm_ * aa_ref[...] * sa * (1.0 - sa)], axis=1).astype(BF16)
            dmh_ref[...] = (dm_ * sh).astype(BF16)
            dma_ref[...] = (dm_ * sa).astype(BF16)

    lat = pl.BlockSpec((TM, D), lambda i: (jnp.maximum(i - 1, 0), 0))
    return _pcall(body, name="merge_bwd", grid=(tt // TM,),
                  in_specs=[lat, lat, lat, pl.BlockSpec((TM, D), lambda i: (i, 2)),
                            pl.BlockSpec((TM, D), lambda i: (i, 3))],
                  out_specs=[pl.BlockSpec((TM, 2 * D), lambda i: (i, C_GATES)), lat, lat],
                  out_shape=[jax.ShapeDtypeStruct((tt, NCOL), BF16), jax.ShapeDtypeStruct((s_len, D), BF16),
                             jax.ShapeDtypeStruct((s_len, D), BF16)], carry=carry)(dm, ah, aa, p, p)


def _local_step(x, ctx, tgt, mod, modc, nw1, nw2, lg, hw, qnw, knw, sinks,
                w_in, wts, dist=None):
    s_len = x.shape[0]
    tt = s_len + L
    tok = jnp.concatenate([ctx, x], axis=0)
    ss1 = jnp.stack([modc, mod[0:2]])
    ss2 = mod[3:5][None]
    g1, g2 = mod[2:3], mod[5:6]
    hw4 = jnp.tile(hw, (1, 4))
    qnw8 = jnp.tile(qnw, (1, 8))
    knw2 = jnp.tile(knw, (1, 2))
    cos, sin = _rope_tables(s_len)
    bd512, bd128 = _blockdiag(ATW, HDIM), _blockdiag(128, HDIM)
    dupm = _dup_matrix()
    dup, dupt = jnp.asarray(dupm, BF16), jnp.asarray(dupm.T, F32)
    tmt = tt

    def four(b):
        return b.reshape(4, 2 * b.shape[1], b.shape[2])

    def halves(g):
        return g.reshape(4, 2, g.shape[1] // 2, g.shape[2])

    h = _modulate(tok, nw1, ss1, name="mod1", sel=lambda i: jnp.minimum(i, 1))
    if dist is None:
        bh4, ba4, w_o, g4, u4, dn4 = wts
        p = _mm_in(h, w_in, tmt)
        o0, st0 = _hgrn_fwd(p, lg, rev=False)
        o1, st1 = _hgrn_fwd(p, lg, rev=True)
    else:
        core, chip = dist
        half = wts[3].shape[1] // 2
        p, first = _mm_in(h, w_in, tmt, carry=_carry_join(_carry_gather(list(wts[0:3])),
                                                          _carry_gather([wts[3]], rows=[(0, half)])))
        (o0, st0), (g8,) = _hgrn_fwd(p, lg, rev=False, carry=_carry_gather([first[3]], rows=[(half, half)]))
        (o1, st1), (u8,) = _hgrn_fwd(p, lg, rev=True, carry=_carry_gather([wts[4]]))
        bh4, ba4, w_o = four(first[0]), four(first[1]), four(first[2]).reshape(D, D)
        g4, u4 = four(g8), four(u8)
    y_hg = _readout(o0, o1, p, hw4)
    qr, k4, v4 = _qk_prep(p, cos, sin, qnw8, knw2, bd512, bd128, dup)
    if dist is None:
        y_at, lse = _attn_fwd(qr, k4, v4, sinks)
    else:
        (y_at, lse), (dn8,) = _attn_fwd(qr, k4, v4, sinks, carry=_carry_gather([wts[5]]))
        dn4 = four(dn8)
    ah = _mm_cs(y_hg, bh4, name="mm_bh")
    aa = _mm_cs(y_at, ba4, name="mm_ba")
    mixed = _merge(ah, aa, p)
    ao = _mm(mixed, w_o, name="mm_o", tm=512, tn=D, tk=D)
    x1, h2 = _res1_mod2(x, ao, g1, nw2, ss2)
    a4, b4, z4 = _ffn_up(h2, g4, u4)
    y = _ffn_down(z4, dn4)
    sq, dx2, dyb, dg2 = _loss_head(x1, y, g2, tgt)

    da4, db4 = _ffn_dz(dyb, dn4, a4, b4)
    g_dn = _ffn_gdn(z4, dyb)
    dh2 = _ffn_dh2(da4, db4, g4, u4)
    g_g, g_u = _ffn_ggu(h2, da4, db4)
    dx1, dattn, dss2, dnw2, dg1 = _mod2_bwd(x1, dh2, dx2, ao, nw2, ss2, g1)
    dm = _mm(dattn, w_o, name="mm_dm", mode="nt", tm=512, tn=D, tk=D)
    g_o = _mm(mixed, dattn, name="mm_go", mode="tn", tm=D, tn=D, tk=512)
    if dist is None:
        dp, dmh, dma = _merge_bwd(dm, ah, aa, p)
    else:
        ffn_units = [halves(g_dn), halves(g_g), halves(g_u)]
        (dp, dmh, dma), ffn_recv = _merge_bwd(dm, ah, aa, p, carry=_carry_pairx(ffn_units))
        ffn_pairs = _rs_pair_add(ffn_units, ffn_recv, core)
    dy_hg = _mm_cs_nt(dmh, bh4, name="mm_dyh")
    dy_at = _mm_cs_nt(dma, ba4, name="mm_dya")
    g_bh = _mm_cs_tn(y_hg, dmh, D // 4, name="mm_gbh")
    g_ba = _mm_cs_tn(y_at, dma, D // 4, name="mm_gba")
    if dist is None:
        dp, do, dhw4 = _readout_bwd(o0, o1, p, hw4, dy_hg, dp)
        dq, dkw, dvw, dkc, dvc, dsk = _attn_bwd(qr, k4, v4, sinks, y_at, lse, dy_at)
        dp, dqnw8, dknw2 = _attn_post(p, cos, sin, qnw8, knw2, bd512, bd128, dupt, dq, dkw, dvw, dkc, dvc, dp)
    else:
        mix_units = [halves(g_bh), halves(g_ba), halves(g_o.reshape(4, D // 4, D))]
        (dp, do, dhw4), mix_recv = _readout_bwd(o0, o1, p, hw4, dy_hg, dp, carry=_carry_pairx(mix_units))
        mix_pairs = _rs_pair_add(mix_units, mix_recv, core)
        (dq, dkw, dvw, dkc, dvc, dsk), c_dn = _attn_bwd(qr, k4, v4, sinks, y_at, lse, dy_at,
                                                        carry=_carry_chipx(ffn_pairs[0:1]))
        red_dn = _rs_chip_add(ffn_pairs[0:1], c_dn, core, chip)
        (dp, dqnw8, dknw2), post = _attn_post(
            p, cos, sin, qnw8, knw2, bd512, bd128, dupt, dq, dkw, dvw, dkc, dvc, dp,
            carry=_carry_join(_carry_chipx(ffn_pairs[1:2]), _carry_sibx(red_dn)))
        red_g = _rs_chip_add(ffn_pairs[1:2], post[0:1], core, chip)
    if dist is None:
        dp, dv0, dq0, dlg0 = _hgrn_bwd(p, lg, do, st0, dp, None, rev=False)
        dp, dlg1 = _hgrn_bwd(p, lg, do, st1, dp, (dv0, dq0), rev=True)
    else:
        (dp, dv0, dq0, dlg0), mid = _hgrn_bwd(p, lg, do, st0, dp, None, rev=False,
                                              carry=_carry_join(_carry_chipx(ffn_pairs[2:3]), _carry_sibx(red_g)))
        red_u = _rs_chip_add(ffn_pairs[2:3], mid[0:1], core, chip)
        (dp, dlg1), last = _hgrn_bwd(p, lg, do, st1, dp, (dv0, dq0), rev=True,
                                     carry=_carry_join(_carry_chipx(mix_pairs), _carry_sibx(red_u)))
        mix_reds = _rs_chip_add(mix_pairs, last[0:3], core, chip)
        ffn_done = post[1:2] + mid[1:2] + last[3:4]
    if dist is None:
        dh = _mm_dh(dp, w_in, tmt)
    else:
        dh, mix_done = _mm_dh(dp, w_in, tmt, carry=_carry_sibx(mix_reds))
    g_in = _mm_gin(dp, h, tmt)
    if dist is None:
        gx, dss1, dnw1 = _mod1_bwd(tok, dh, dx1, nw1, ss1)
        rs = None
    else:
        in_units = [halves(g_in.reshape(4, NCOL // 4, D))]
        (gx, dss1, dnw1), in_recv = _mod1_bwd(tok, dh, dx1, nw1, ss1, carry=_carry_pairx(in_units))
        rs = dict(ffn_done=ffn_done, mix_done=mix_done, in_units=in_units, in_recv=in_recv)

    dmod = jnp.concatenate([dss1[1], dg1, dss2, dg2], axis=0)
    dmodc = dss1[0]
    raw = (dss1, dg1, dss2, dg2, dnw1, dnw2, dhw4, dqnw8, dknw2, dsk, dlg0, dlg1)
    small = dict(raw=raw, dmod=dmod, dmodc=dmodc, dnw1=dnw1, dnw2=dnw2,
                 dhw=dhw4.reshape(4, HGD).sum(0, keepdims=True),
                 dqnw=dqnw8.reshape(8, HDIM).sum(0, keepdims=True),
                 dknw=dknw2.reshape(2, HDIM).sum(0, keepdims=True),
                 dsinks=dsk[:, 0], dlg=jnp.concatenate([dlg0, dlg1], axis=0))
    big = dict(w_in=g_in, w_bh=g_bh, w_ba=g_ba, w_o=g_o, w_g=g_g, w_u=g_u, w_dn=g_dn)
    return sq, gx, big, small, rs


def _place():
    x, y, c = lax.axis_index("x"), lax.axis_index("y"), lax.axis_index("c")
    return x, y, c


def _gather_blocks(x_refs, out_refs, send_sems, recv_sems, local_sems):
    n = len(out_refs)
    x, y, c = _place()
    me, sibling = (x, y, c), (x, y, 1 - c)
    chips = [(1 - x, y), (x, 1 - y), (1 - x, 1 - y)]

    def slot(u, px, py, pc):
        return out_refs[u].at[4 * px + 2 * py + pc]

    def copy(u, k, block, to, src=None):
        return pltpu.make_async_remote_copy(
            src_ref=slot(u, *block) if src is None else src, dst_ref=slot(u, *block),
            send_sem=send_sems.at[u, k], recv_sem=recv_sems.at[u, k], device_id=to, device_id_type=MESH)

    mines = []
    if x_refs is not None:
        mines = [pltpu.make_async_copy(x_refs[u], slot(u, *me), local_sems.at[u]) for u in range(n)]
    for cp in mines:
        cp.start()
    first = []
    for u in range(n):
        src = None if x_refs is None else x_refs[u]
        first.append(copy(u, 0, me, sibling, src=src))
        first += [copy(u, 1 + j, me, (*chip, c), src=src) for j, chip in enumerate(chips)]
    for cp in first:
        cp.start()
    passed = []
    for j, chip in enumerate(chips):
        for u in range(n):
            copy(u, 1 + j, (*chip, c), me).wait_recv()
            fwd = copy(u, 4 + j, (*chip, c), sibling)
            fwd.start()
            passed.append(fwd)
    for u in range(n):
        copy(u, 0, sibling, me).wait_recv()
    for j, chip in enumerate(chips):
        for u in range(n):
            copy(u, 4 + j, (*chip, 1 - c), me).wait_recv()
    for cp in first + passed:
        cp.wait_send()
    for cp in mines:
        cp.wait()


def _gather_sems(n):
    return [pltpu.SemaphoreType.DMA((n, 7)), pltpu.SemaphoreType.DMA((n, 7)), pltpu.SemaphoreType.DMA((n,))]


def _allgather(blks, *, name, in_vmem):
    n = len(blks)
    space = pltpu.VMEM if in_vmem else pl.ANY

    def body(*refs):
        _gather_blocks(refs[:n], refs[n:2 * n], *refs[2 * n:])

    return pl.pallas_call(
        body, name=name, out_shape=[jax.ShapeDtypeStruct((8,) + b.shape, b.dtype) for b in blks],
        in_specs=[pl.BlockSpec(memory_space=space)] * n, out_specs=[pl.BlockSpec(memory_space=space)] * n,
        scratch_shapes=_gather_sems(n))(*blks)


def _cast_place(ws, c, dev):
    n = len(ws)

    def body(s_ref, *refs):
        for u in range(n):
            refs[n + u][0] = refs[u][...].astype(BF16)

    in_specs, out_specs, out_shape = [], [], []
    for w in ws:
        q, cols = w.shape[0] // 4, w.shape[1]
        in_specs.append(pl.BlockSpec((q, cols), lambda i, s: (2 * s[0] + i, 0)))
        out_specs.append(pl.BlockSpec((1, q, cols), lambda i, s: (s[1], i, 0)))
        out_shape.append(jax.ShapeDtypeStruct((8, 2 * q, cols), BF16))
    return pl.pallas_call(
        body, name="cast_place",
        grid_spec=pltpu.PrefetchScalarGridSpec(num_scalar_prefetch=1, grid=(2,), in_specs=in_specs,
                                               out_specs=out_specs),
        out_shape=out_shape,
        compiler_params=pltpu.CompilerParams(vmem_limit_bytes=48 << 20))(jnp.stack([c, dev]), *ws)


def _gather_phases(out_refs, send_sems, recv_sems, rows=None):
    n = len(out_refs)
    x, y, c = _place()
    me, sibling = (x, y, c), (x, y, 1 - c)
    chips = [(1 - x, y), (x, 1 - y), (1 - x, 1 - y)]

    def copy(u, k, block, to):
        px, py, pc = block
        ref = out_refs[u].at[4 * px + 2 * py + pc]
        if rows is not None and rows[u] is not None:
            ref = ref.at[pl.ds(rows[u][0], rows[u][1])]
        return pltpu.make_async_remote_copy(src_ref=ref, dst_ref=ref, send_sem=send_sems.at[u, k],
                                            recv_sem=recv_sems.at[u, k], device_id=to, device_id_type=MESH)

    def start():
        for u in range(n):
            copy(u, 0, me, sibling).start()
            for j, chip in enumerate(chips):
                copy(u, 1 + j, me, (*chip, c)).start()

    def mid():
        for j, chip in enumerate(chips):
            for u in range(n):
                copy(u, 1 + j, (*chip, c), me).wait_recv()
                copy(u, 4 + j, (*chip, c), sibling).start()

    def end():
        for u in range(n):
            copy(u, 0, sibling, me).wait_recv()
        for j, chip in enumerate(chips):
            for u in range(n):
                copy(u, 4 + j, (*chip, 1 - c), me).wait_recv()
        for u in range(n):
            copy(u, 0, me, sibling).wait_send()
            for j, chip in enumerate(chips):
                copy(u, 1 + j, me, (*chip, c)).wait_send()
                copy(u, 4 + j, (*chip, c), sibling).wait_send()

    return start, mid, end


def _carry_gather(bufs, rows=None):
    n = len(bufs)
    return _Carry(bufs, [jax.ShapeDtypeStruct(b.shape, b.dtype) for b in bufs], {u: u for u in range(n)},
                  [pltpu.SemaphoreType.DMA((n, 7)), pltpu.SemaphoreType.DMA((n, 7))],
                  lambda ins, outs, sems: _gather_phases(outs, *sems, rows=rows))


def _allgather_inplace(bufs, *, name):
    n = len(bufs)

    def body(*refs):
        for phase in _gather_phases(refs[n:2 * n], *refs[2 * n:]):
            phase()

    return pl.pallas_call(
        body, name=name, out_shape=[jax.ShapeDtypeStruct(b.shape, b.dtype) for b in bufs],
        in_specs=[ANY] * n, out_specs=[ANY] * n, input_output_aliases={u: u for u in range(n)},
        scratch_shapes=[pltpu.SemaphoreType.DMA((n, 7)), pltpu.SemaphoreType.DMA((n, 7))])(*bufs)


def _ag_small(raw):
    def body(dss1, dg1, dss2, dg2, dnw1, dnw2, dhw4, dqnw8, dknw2, dsk, dlg0, dlg1,
             out_ref, tot_ref, blk, send_sems, recv_sems, local_sems):
        blk[...] = jnp.zeros_like(blk)
        blk[0:2, :] = dss1[1]
        blk[2:3, :] = dg1[...]
        blk[3:5, :] = dss2[...]
        blk[5:6, :] = dg2[...]
        blk[6:8, :] = dss1[0]
        blk[8:9, :] = dnw1[...]
        blk[9:10, :] = dnw2[...]
        blk[10:11, 0:HGW] = dhw4[...]
        blk[10:11, HGW:D] = dqnw8[...]
        blk[11:12, 0:128] = dknw2[...]
        blk[12:14, 0:HGW] = dlg0[0]
        blk[14:16, 0:HGW] = dlg1[0]
        blk[16:24, 0:128] = dsk[...]
        _gather_blocks([blk], [out_ref], send_sems, recv_sems, local_sems)
        acc = out_ref[0]
        for i in range(1, 8):
            acc = acc + out_ref[i]
        tot_ref[...] = acc

    vm = pl.BlockSpec(memory_space=pltpu.VMEM)
    return pl.pallas_call(
        body, name="ag_small",
        out_shape=[jax.ShapeDtypeStruct((8, 24, D), F32), jax.ShapeDtypeStruct((24, D), F32)],
        in_specs=[vm] * 12, out_specs=[vm, vm],
        scratch_shapes=[pltpu.VMEM((24, D), F32)] + _gather_sems(1))(*raw)


def _rs_pair_exchange(units):
    n = len(units)

    def body(*refs):
        start, _, end = _pairx_phases(refs[:n], refs[n:2 * n], *refs[2 * n:])
        start()
        end()

    return pl.pallas_call(
        body, name="rs_pair_exchange", out_shape=_pairx_shapes(units),
        in_specs=[ANY] * n, out_specs=[ANY] * n,
        scratch_shapes=[pltpu.SemaphoreType.DMA((n, 4)), pltpu.SemaphoreType.DMA((n, 4))])(*units)


def _pairx_shapes(units):
    return [jax.ShapeDtypeStruct((4,) + g.shape[2:], g.dtype) for g in units]


def _pairx_phases(g_refs, r_refs, send_sems, recv_sems):
    n = len(g_refs)
    x, y, c = _place()
    cps = [pltpu.make_async_remote_copy(
        src_ref=g_refs[u].at[j, 1 - c], dst_ref=r_refs[u].at[j], send_sem=send_sems.at[u, j],
        recv_sem=recv_sems.at[u, j], device_id=(x, y, 1 - c), device_id_type=MESH)
        for u in range(n) for j in range(4)]

    def start():
        for cp in cps:
            cp.start()

    def end():
        for cp in cps:
            cp.wait()

    return start, None, end


def _carry_pairx(units):
    n = len(units)
    return _Carry(units, _pairx_shapes(units), {},
                  [pltpu.SemaphoreType.DMA((n, 4)), pltpu.SemaphoreType.DMA((n, 4))],
                  lambda ins, outs, sems: _pairx_phases(ins, outs, *sems))


def _rs_pair_add(units, recvs, c):
    n = len(units)

    def body(c_ref, *refs):
        for u in range(n):
            refs[2 * n + u][...] = (refs[u][0] + refs[n + u][...]).astype(BF16)

    in_specs, out_specs, out_shape = [], [], []
    for g in units:
        h, w = g.shape[2] // 2, g.shape[3]
        in_specs.append(pl.BlockSpec((1, 1, h, w), lambda j, i, cr: (j, cr[0], i, 0)))
    for g in units:
        h, w = g.shape[2] // 2, g.shape[3]
        in_specs.append(pl.BlockSpec((1, h, w), lambda j, i, cr: (j, i, 0)))
        out_specs.append(pl.BlockSpec((1, h, w), lambda j, i, cr: (j, i, 0)))
        out_shape.append(jax.ShapeDtypeStruct((4, 2 * h, w), BF16))
    return pl.pallas_call(
        body, name="rs_pair_add",
        grid_spec=pltpu.PrefetchScalarGridSpec(num_scalar_prefetch=1, grid=(4, 2), in_specs=in_specs,
                                               out_specs=out_specs),
        out_shape=out_shape,
        compiler_params=pltpu.CompilerParams(vmem_limit_bytes=48 << 20))(c.reshape(1), *units, *recvs)


def _rs_chip_exchange(pairs):
    n = len(pairs)

    def body(*refs):
        start, _, end = _chipx_phases(refs[:n], refs[n:2 * n], *refs[2 * n:])
        start()
        end()

    return pl.pallas_call(
        body, name="rs_chip_exchange", out_shape=[jax.ShapeDtypeStruct(p.shape, p.dtype) for p in pairs],
        in_specs=[ANY] * n, out_specs=[ANY] * n,
        scratch_shapes=[pltpu.SemaphoreType.DMA((n, 3)), pltpu.SemaphoreType.DMA((n, 3))])(*pairs)


def _chipx_phases(p_refs, r_refs, send_sems, recv_sems):
    n = len(p_refs)
    x, y, c = _place()
    k = 2 * x + y
    sends = []
    for d in range(1, 4):
        j = (k + d) % 4
        for u in range(n):
            sends.append(pltpu.make_async_remote_copy(
                src_ref=p_refs[u].at[j], dst_ref=r_refs[u].at[k], send_sem=send_sems.at[u, d - 1],
                recv_sem=recv_sems.at[u, d - 1], device_id=(j // 2, j % 2, c), device_id_type=MESH))

    def start():
        for cp in sends:
            cp.start()

    def end():
        for d in range(1, 4):
            src = (k + 4 - d) % 4
            for u in range(n):
                pltpu.make_async_remote_copy(
                    src_ref=p_refs[u].at[src], dst_ref=r_refs[u].at[src], send_sem=send_sems.at[u, d - 1],
                    recv_sem=recv_sems.at[u, d - 1], device_id=(x, y, c), device_id_type=MESH).wait_recv()
        for cp in sends:
            cp.wait_send()

    return start, None, end


def _carry_chipx(pairs):
    n = len(pairs)
    return _Carry(pairs, [jax.ShapeDtypeStruct(p.shape, p.dtype) for p in pairs], {},
                  [pltpu.SemaphoreType.DMA((n, 3)), pltpu.SemaphoreType.DMA((n, 3))],
                  lambda ins, outs, sems: _chipx_phases(ins, outs, *sems))


def _rs_chip_add(pairs, contribs, c, chip):
    n = len(pairs)

    def body(s_ref, *refs):
        for u in range(n):
            a, b, c_, d = refs[4 * u:4 * u + 4]
            refs[4 * n + u][0] = ((a[0].astype(F32) + b[0].astype(F32)) + c_[0].astype(F32)) + d[0].astype(F32)

    in_specs, out_specs, out_shape, args = [], [], [], []
    for p, r in zip(pairs, contribs):
        h, w = p.shape[1] // 2, p.shape[2]
        in_specs += [pl.BlockSpec((1, h, w), functools.partial(lambda d, i, s: ((s[1] + d) % 4, i, 0), d))
                     for d in range(4)]
        args += [p, r, r, r]
        out_specs.append(pl.BlockSpec((1, h, w), lambda i, s: (s[0], i, 0)))
        out_shape.append(jax.ShapeDtypeStruct((2, 2 * h, w), F32))
    return pl.pallas_call(
        body, name="rs_chip_add",
        grid_spec=pltpu.PrefetchScalarGridSpec(num_scalar_prefetch=1, grid=(2,), in_specs=in_specs,
                                               out_specs=out_specs),
        out_shape=out_shape,
        compiler_params=pltpu.CompilerParams(vmem_limit_bytes=48 << 20))(jnp.stack([c, chip]), *args)


def _rs_sibling_gather(reds):
    n = len(reds)

    def body(*refs):
        start, _, end = _sibx_phases(refs[n:2 * n], *refs[2 * n:])
        start()
        end()

    return pl.pallas_call(
        body, name="rs_sibling_gather", out_shape=[jax.ShapeDtypeStruct(r.shape, r.dtype) for r in reds],
        in_specs=[ANY] * n, out_specs=[ANY] * n, input_output_aliases={u: u for u in range(n)},
        scratch_shapes=[pltpu.SemaphoreType.DMA((n,))] * 2)(*reds)


def _sibx_phases(o_refs, send_sems, recv_sems):
    n = len(o_refs)
    x, y, c = _place()
    cps = [pltpu.make_async_remote_copy(
        src_ref=o_refs[u].at[c], dst_ref=o_refs[u].at[c], send_sem=send_sems.at[u], recv_sem=recv_sems.at[u],
        device_id=(x, y, 1 - c), device_id_type=MESH) for u in range(n)]

    def start():
        for cp in cps:
            cp.start()

    def end():
        for u in range(n):
            cps[u].wait_send()
            pltpu.make_async_remote_copy(
                src_ref=o_refs[u].at[1 - c], dst_ref=o_refs[u].at[1 - c], send_sem=send_sems.at[u],
                recv_sem=recv_sems.at[u], device_id=(x, y, 1 - c), device_id_type=MESH).wait_recv()

    return start, None, end


def _carry_sibx(reds):
    n = len(reds)
    return _Carry(reds, [jax.ShapeDtypeStruct(r.shape, r.dtype) for r in reds], {u: u for u in range(n)},
                  [pltpu.SemaphoreType.DMA((n,))] * 2, lambda ins, outs, sems: _sibx_phases(outs, *sems))


def _ada_fwd(c16, w, b):
    n = w.shape[1]
    tn = 512

    def body(c_ref, w_ref, b_ref, o_ref):
        cc = c_ref[...]
        o_ref[...] = _dot(cc * _sig(cc), w_ref[...], prec=HI) + b_ref[...]

    return _pcall(body, name="ada_fwd", grid=(n // tn,),
                  in_specs=[_full((16, D)), pl.BlockSpec((D, tn), lambda j: (0, j)),
                            pl.BlockSpec((1, tn), lambda j: (0, j))],
                  out_specs=pl.BlockSpec((16, tn), lambda j: (0, j)),
                  out_shape=jax.ShapeDtypeStruct((16, n), F32))(c16, w, b)


def _ada_bwd(c16, dmod16, w):
    n = w.shape[1]
    tn = 512

    def body(c_ref, d_ref, w_ref, gw_ref, gc_ref):
        j = pl.program_id(0)

        @pl.when(j == 0)
        def _():
            gc_ref[...] = jnp.zeros_like(gc_ref)

        cc = c_ref[...]
        dm = d_ref[...]
        gw_ref[...] = _dot(cc * _sig(cc), dm, TN, prec=HI)
        gc_ref[...] += _dot(dm, w_ref[...], NT, prec=HI)

    return _pcall(body, name="ada_bwd", grid=(n // tn,),
                  in_specs=[_full((16, D)), pl.BlockSpec((16, tn), lambda j: (0, j)),
                            pl.BlockSpec((D, tn), lambda j: (0, j))],
                  out_specs=[pl.BlockSpec((D, tn), lambda j: (0, j)), _full((16, D))],
                  out_shape=[jax.ShapeDtypeStruct((D, n), F32),
                             jax.ShapeDtypeStruct((16, D), F32)])(c16, dmod16, w)


def _adam_math(w, g, m, v):
    c1 = 1.0 - ADAM_B1 ** ADAM_STEP
    c2 = 1.0 - ADAM_B2 ** ADAM_STEP
    nm = ADAM_B1 * m + (1.0 - ADAM_B1) * g
    nv = ADAM_B2 * v + (1.0 - ADAM_B2) * (g * g)
    return -ADAM_LR * ((nm / c1) / (jnp.sqrt(nv / c2) + ADAM_EPS) + ADAM_WD * w), nm, nv


def _adamw_small(ws, gs, ms, vs):
    n = len(ws)

    def body(*refs):
        for u in range(n):
            d_, nm, nv = _adam_math(refs[u][...], refs[n + u][...], refs[2 * n + u][...], refs[3 * n + u][...])
            refs[4 * n + u][...] = d_
            refs[5 * n + u][...] = nm
            refs[6 * n + u][...] = nv

    specs = [_full(w.shape) for w in ws]
    shapes = [jax.ShapeDtypeStruct(w.shape, F32) for w in ws]
    out = _pcall(body, name="adamw_small", grid=(1,), in_specs=specs * 4, out_specs=specs * 3,
                 out_shape=shapes * 3)(*ws, *gs, *ms, *vs)
    return out[:n], out[n:2 * n], out[2 * n:]


def _cctx_grad(parts, c_ctx):
    def body(p_ref, c_ref, o_ref):
        acc = p_ref[0:1, :]
        for k in range(1, 4):
            acc = acc + p_ref[k:k + 1, :]
        cc = c_ref[...]
        s = _sig(cc)
        o_ref[...] = acc * (s * (1.0 + cc * (1.0 - s)))

    return _pcall(body, name="cctx_grad", grid=(1,), in_specs=[_full(parts.shape), _full((1, D))],
                  out_specs=_full((1, D)), out_shape=jax.ShapeDtypeStruct((1, D), F32))(parts, c_ctx)


ADAM_STEPS = 8


def _adamw_multi(ws, gs, ms, vs, *, name, carry=None):
    n = len(ws)

    def body(*refs):
        for u in range(n):
            refs[4 * n + u][...], refs[5 * n + u][...], refs[6 * n + u][...] = _adam_math(
                refs[u][...], refs[n + u][...], refs[2 * n + u][...], refs[3 * n + u][...])

    specs = [pl.BlockSpec((w.shape[0] // ADAM_STEPS, w.shape[1]), lambda i: (i, 0)) for w in ws]
    shapes = [jax.ShapeDtypeStruct(w.shape, F32) for w in ws]
    res = _pcall(body, name=name, grid=(ADAM_STEPS,), in_specs=specs * 4, out_specs=specs * 3,
                 out_shape=shapes * 3, carry=carry)(*ws, *gs, *ms, *vs)
    out, extra = res if carry is not None else (res, None)
    return (out[:n], out[n:2 * n], out[2 * n:]), extra


def kernel(x, c, ctx, c_ctx, w_ada, b_ada, norm_mix_w, norm_ffn_w, w_in, hgrn_lb_logits, hgrn_norm_w, q_norm_w, k_norm_w, attn_sinks, w_branch_hgrn, w_branch_attn, w_out, w_ffn_gate, w_ffn_up, w_ffn_down, loss_target, m_c_ctx, m_w_ada, m_b_ada, m_norm_mix_w, m_norm_ffn_w, m_w_in, m_hgrn_lb_logits, m_hgrn_norm_w, m_q_norm_w, m_k_norm_w, m_attn_sinks, m_w_branch_hgrn, m_w_branch_attn, m_w_out, m_w_ffn_gate, m_w_ffn_up, m_w_ffn_down, v_c_ctx, v_w_ada, v_b_ada, v_norm_mix_w, v_norm_ffn_w, v_w_in, v_hgrn_lb_logits, v_hgrn_norm_w, v_q_norm_w, v_k_norm_w, v_attn_sinks, v_w_branch_hgrn, v_w_branch_attn, v_w_out, v_w_ffn_gate, v_w_ffn_up, v_w_ffn_down):
    xi, yi, ci = _place()
    chip = 2 * xi + yi
    dev = 2 * chip + ci
    s_len = x.shape[1]

    lbrow = jnp.pad(hgrn_lb_logits.reshape(1, 512), ((0, 0), (0, D - 512)))
    blk = jnp.concatenate([c, lbrow, jnp.zeros((6, D), F32)], axis=0)
    g0, = _allgather([blk], name="ag_cond", in_vmem=True)
    c16 = jnp.concatenate([g0[:, 0], c_ctx[None], jnp.zeros((7, D), F32)], axis=0)
    lg = g0[0::2, 1, :512].reshape(4, 2, 2, 128).transpose(1, 2, 0, 3).reshape(2, 2, HGW)

    nada = w_ada.shape[2]
    b_sh = lax.dynamic_slice(b_ada, (0, chip * nada), (1, nada))
    mod_sh = _ada_fwd(c16, w_ada[0], b_sh)
    g1, = _allgather([mod_sh], name="ag_mod", in_vmem=True)
    modall = g1[0::2].transpose(1, 0, 2).reshape(16, 4 * nada)
    mod = lax.dynamic_slice(modall, (dev, 0), (1, 6 * D)).reshape(6, D)
    modc = modall[8].reshape(6, D)[:2]

    shards = [w_in[0].T, w_branch_hgrn[0], w_branch_attn[0], w_out[0], w_ffn_gate[0].T, w_ffn_up[0].T,
              w_ffn_down[0]]
    bufs = _cast_place(shards, ci, dev)
    in8, = _allgather_inplace(bufs[0:1], name="ag_w_in")

    sq, gx, _, small, rs = _local_step(
        x[0], ctx[0], loss_target[0], mod, modc, norm_mix_w, norm_ffn_w, lg, hgrn_norm_w, q_norm_w,
        k_norm_w, attn_sinks[0], in8.reshape(NCOL, D), bufs[1:], dist=(ci, chip))
    loss = lax.psum(0.5 * jnp.sum(sq) / D, ("x", "y", "c"))

    def whole(r):
        return r.reshape(2 * r.shape[1], r.shape[2])

    g_dn, g_g, g_u = [whole(r) for r in rs["ffn_done"]]
    g_bh, g_ba, g_o = [whole(r) for r in rs["mix_done"]]
    in_pairs = _rs_pair_add(rs["in_units"], rs["in_recv"], ci)

    g2, tot = _ag_small(small["raw"])
    dmodc_tot = jnp.pad(tot[6:8].reshape(1, 2 * D), ((0, 0), (0, 4 * D)))
    g_b_ada = tot[0:6].reshape(1, 6 * D) + dmodc_tot
    dmod16 = jnp.concatenate([g2[:, 0:6].reshape(8, 6 * D), dmodc_tot, jnp.zeros((7, 6 * D), F32)], axis=0)
    g_w_ada, gc_part = _ada_bwd(c16, lax.dynamic_slice(dmod16, (0, chip * nada), (16, nada)), w_ada[0])
    g3, = _allgather([gc_part[8:16]], name="ag_cctx", in_vmem=True)
    g_c_ctx = _cctx_grad(g3[0::2, 0], c_ctx[None])[0]
    g_nw1 = tot[8:9]
    g_nw2 = tot[9:10]
    g_hw = tot[10, :HGW].reshape(4, HGD).sum(0, keepdims=True)
    g_qnw = tot[10, HGW:].reshape(8, HDIM).sum(0, keepdims=True)
    g_knw = tot[11, :128].reshape(2, HDIM).sum(0, keepdims=True)
    g_sinks = tot[16:24, 0][None]
    g_lg = lax.dynamic_slice(tot[12:16, :HGW].reshape(2, 2, HGW), (0, 0, chip * 128), (2, 2, 128))

    names = ["c_ctx", "w_ada", "b_ada", "norm_mix_w", "norm_ffn_w", "w_in", "hgrn_lb_logits", "hgrn_norm_w",
             "q_norm_w", "k_norm_w", "attn_sinks", "w_branch_hgrn", "w_branch_attn", "w_out", "w_ffn_gate",
             "w_ffn_up", "w_ffn_down"]
    ws = dict(zip(names, [c_ctx, w_ada, b_ada, norm_mix_w, norm_ffn_w, w_in, hgrn_lb_logits, hgrn_norm_w,
                          q_norm_w, k_norm_w, attn_sinks, w_branch_hgrn, w_branch_attn, w_out, w_ffn_gate,
                          w_ffn_up, w_ffn_down]))
    ms = dict(zip(names, [m_c_ctx, m_w_ada, m_b_ada, m_norm_mix_w, m_norm_ffn_w, m_w_in, m_hgrn_lb_logits,
                          m_hgrn_norm_w, m_q_norm_w, m_k_norm_w, m_attn_sinks, m_w_branch_hgrn,
                          m_w_branch_attn, m_w_out, m_w_ffn_gate, m_w_ffn_up, m_w_ffn_down]))
    vs = dict(zip(names, [v_c_ctx, v_w_ada, v_b_ada, v_norm_mix_w, v_norm_ffn_w, v_w_in, v_hgrn_lb_logits,
                          v_hgrn_norm_w, v_q_norm_w, v_k_norm_w, v_attn_sinks, v_w_branch_hgrn,
                          v_w_branch_attn, v_w_out, v_w_ffn_gate, v_w_ffn_up, v_w_ffn_down]))
    transposed = ("w_in", "w_ffn_gate", "w_ffn_up")

    def view(a, n):
        return a[0].T if n in transposed else a[0]

    def unview(a, n):
        return a.T[None] if n in transposed else a[None]

    delta, new_m, new_v, grads = {}, {}, {}, {}

    def big_adamw(group, gs, name, carry=None):
        (d_, m_, v_), extra = _adamw_multi([view(ws[n], n) for n in group], gs, [view(ms[n], n) for n in group],
                                           [view(vs[n], n) for n in group], name=name, carry=carry)
        for i, n in enumerate(group):
            grads[n], delta[n], new_m[n], new_v[n] = (unview(gs[i], n), unview(d_[i], n), unview(m_[i], n),
                                                      unview(v_[i], n))
        return extra

    in_contribs = big_adamw(["w_ffn_down", "w_ffn_gate", "w_ffn_up", "w_out", "w_branch_hgrn", "w_branch_attn"],
                            [g_dn, g_g, g_u, g_o, g_bh, g_ba], "adamw_first", carry=_carry_chipx(in_pairs))
    in_reds = _rs_chip_add(in_pairs, in_contribs, ci, chip)
    g_in, = [whole(r) for r in _rs_sibling_gather(in_reds)]
    big_adamw(["w_in", "w_ada"], [g_in, g_w_ada], "adamw_second")
    grads.update(c_ctx=g_c_ctx, b_ada=g_b_ada, norm_mix_w=g_nw1, norm_ffn_w=g_nw2, hgrn_lb_logits=g_lg,
                 hgrn_norm_w=g_hw, q_norm_w=g_qnw, k_norm_w=g_knw, attn_sinks=g_sinks)
    small_names = [n for n in names if n not in delta]

    def two_d(a):
        return a.reshape(1, -1) if a.ndim == 1 else a

    sd, sm_, sv = _adamw_small(*[[two_d(d[n]) for n in small_names] for d in (ws, grads, ms, vs)])
    for i, n in enumerate(small_names):
        for dst, src in ((delta, sd), (new_m, sm_), (new_v, sv)):
            dst[n] = src[i].reshape(ws[n].shape)
    return (loss, gx[None], *[grads[n] for n in names], *[delta[n] for n in names],
            *[new_m[n] for n in names], *[new_v[n] for n in names])
```

```python
import functools

import numpy as np
import jax
import jax.numpy as jnp
from jax import lax
from jax.experimental import pallas as pl
from jax.experimental.pallas import tpu as pltpu

F32 = jnp.float32
BF16 = jnp.bfloat16
HI = lax.Precision.HIGHEST
MESH = pl.DeviceIdType.MESH

D = 1024
L = 256
TM = 256
HGW = 512
HGD = 128
CH = 32
ATW = 512
HDIM = 64
BLK = 128
GRID_W = 64
DFF = 2816
NCOL = 5376
EPS = 1e-6
ROPE_THETA = 10000.0

C_FB, C_INP, C_QHG, C_FF = 0, 1, 2, 3
C_GATES = 1
C_GHG, C_QRAW = 8, 9
C_KV = 20
C_QKV = 6

ADAM_LR, ADAM_B1, ADAM_B2, ADAM_EPS, ADAM_WD, ADAM_STEP = 0.001, 0.9, 0.999, 1e-08, 0.01, 10

NN = (((1,), (0,)), ((), ()))
NT = (((1,), (1,)), ((), ()))
TN = (((0,), (0,)), ((), ()))


def _dot(a, b, dims=NN, prec=None):
    return lax.dot_general(a, b, dims, precision=prec, preferred_element_type=F32)


def _bdot(a, b, dims=NN):
    return _dot(a.astype(BF16), b.astype(BF16), dims)


def _sig(x):
    return 1.0 / (1.0 + jnp.exp(-x))


class _Carry:
    def __init__(self, ins, outs, aliases, scratch, phases):
        self.ins, self.outs, self.aliases, self.scratch, self.phases = ins, outs, aliases, scratch, phases


def _in_hbm(args):
    return [pltpu.with_memory_space_constraint(a, pltpu.HBM) for a in args]


def _carry_join(a, b):
    na_in, na_out, na_sc = len(a.ins), len(a.outs), len(a.scratch)
    aliases = dict(a.aliases)
    aliases.update({na_in + i: na_out + o for i, o in b.aliases.items()})

    def phases(ins, outs, sems):
        pa = a.phases(ins[:na_in], outs[:na_out], sems[:na_sc])
        pb = b.phases(ins[na_in:], outs[na_out:], sems[na_sc:])

        def both(fa, fb):
            if fa is None and fb is None:
                return None

            def run():
                for fn in (fa, fb):
                    if fn is not None:
                        fn()
            return run

        return tuple(both(fa, fb) for fa, fb in zip(pa, pb))

    return _Carry(list(a.ins) + list(b.ins), list(a.outs) + list(b.outs), aliases,
                  list(a.scratch) + list(b.scratch), phases)


def _pcall(body, *, name, grid, in_specs, out_specs, out_shape, scratch=(), aliases=None, vmem_mb=48,
           carry=None):
    params = pltpu.CompilerParams(dimension_semantics=("arbitrary",) * len(grid),
                                  vmem_limit_bytes=vmem_mb << 20)
    if carry is None:
        plain = pl.pallas_call(
            body, name=name, grid=grid, in_specs=in_specs, out_specs=out_specs, out_shape=out_shape,
            scratch_shapes=list(scratch), input_output_aliases=aliases or {}, compiler_params=params)
        return lambda *args: plain(*_in_hbm(args))
    single = not isinstance(out_shape, (list, tuple))
    out_specs_l = [out_specs] if single else list(out_specs)
    out_shape_l = [out_shape] if single else list(out_shape)
    n_in, n_out, n_sc = len(in_specs), len(out_shape_l), len(scratch)
    k_in, k_out = len(carry.ins), len(carry.outs)
    nsteps = int(np.prod(grid))
    assert nsteps >= 3

    def wrapped(*refs):
        ins, cins = refs[:n_in], refs[n_in:n_in + k_in]
        o0 = n_in + k_in
        outs, couts = refs[o0:o0 + n_out], refs[o0 + n_out:o0 + n_out + k_out]
        s0 = o0 + n_out + k_out
        sc, csc = refs[s0:s0 + n_sc], refs[s0 + n_sc:]
        step = pl.program_id(0)
        for ax in range(1, len(grid)):
            step = step * grid[ax] + pl.program_id(ax)
        start, mid, end = carry.phases(cins, couts, csc)
        pl.when(step == 0)(start)
        body(*ins, *outs, *sc)
        if mid is not None:
            pl.when(step == nsteps - 2)(mid)
        pl.when(step == nsteps - 1)(end)

    all_aliases = dict(aliases or {})
    all_aliases.update({n_in + i: n_out + o for i, o in carry.aliases.items()})
    call = pl.pallas_call(
        wrapped, name=name, grid=grid, in_specs=list(in_specs) + [ANY] * k_in,
        out_specs=out_specs_l + [ANY] * k_out, out_shape=out_shape_l + list(carry.outs),
        scratch_shapes=list(scratch) + list(carry.scratch), input_output_aliases=all_aliases,
        compiler_params=params)

    def run(*args):
        res = call(*_in_hbm(args), *carry.ins)
        core = res[:n_out]
        return (core[0] if single else list(core)), list(res[n_out:])

    return run


def _full(shape):
    nd = len(shape)
    return pl.BlockSpec(shape, lambda *_: (0,) * nd)


ANY = pl.BlockSpec(memory_space=pl.ANY)


def _mm(a, b, *, name, mode="nn", out_dtype=F32, tm, tn, tk):
    if mode == "nn":
        (m, k), (k2, n) = a.shape, b.shape
    elif mode == "nt":
        (m, k), (n, k2) = a.shape, b.shape
    else:
        (k, m), (k2, n) = a.shape, b.shape
    assert k == k2 and m % tm == 0 and n % tn == 0 and k % tk == 0, (name, a.shape, b.shape)
    nk = k // tk
    dims = {"nn": NN, "nt": NT, "tn": TN}[mode]

    def body(a_ref, b_ref, o_ref, acc):
        kk = pl.program_id(2)

        @pl.when(kk == 0)
        def _():
            acc[...] = jnp.zeros_like(acc)

        acc[...] += _bdot(a_ref[...], b_ref[...], dims)

        @pl.when(kk == nk - 1)
        def _():
            o_ref[...] = acc[...].astype(out_dtype)

    a_spec = (pl.BlockSpec((tk, tm), lambda i, j, kk: (kk, i)) if mode == "tn"
              else pl.BlockSpec((tm, tk), lambda i, j, kk: (i, kk)))
    b_spec = (pl.BlockSpec((tn, tk), lambda i, j, kk: (j, kk)) if mode == "nt"
              else pl.BlockSpec((tk, tn), lambda i, j, kk: (kk, j)))
    return _pcall(body, name=name, grid=(m // tm, n // tn, nk), in_specs=[a_spec, b_spec],
                  out_specs=pl.BlockSpec((tm, tn), lambda i, j, kk: (i, j)),
                  out_shape=jax.ShapeDtypeStruct((m, n), out_dtype),
                  scratch=[pltpu.VMEM((tm, tn), F32)])(a, b)


NT_IN = NCOL // 256


def _src_block(j):
    return j + jnp.where(j < 4, 2, jnp.where(j < 6, 3, jnp.where(j < 8, -6, jnp.where(
        j < 16, 5, jnp.where(j < 20, -7, -14)))))


def _mm_in(h, wt, tm, carry=None):
    tt = h.shape[0]

    def body(h_ref, w_ref, o_ref):
        o_ref[...] = _bdot(h_ref[...], w_ref[...], NT)

    return _pcall(body, name="mm_in", grid=(tt // tm, NT_IN),
                  in_specs=[pl.BlockSpec((tm, D), lambda i, j: (i, 0)),
                            pl.BlockSpec((256, D), lambda i, j: (_src_block(j), 0))],
                  out_specs=pl.BlockSpec((tm, 256), lambda i, j: (i, j)),
                  out_shape=jax.ShapeDtypeStruct((tt, NCOL), F32), carry=carry)(h, wt)


def _mm_dh(dp, wt, tm, carry=None):
    tt = dp.shape[0]

    def body(d_ref, w_ref, o_ref, acc):
        kk = pl.program_id(1)

        @pl.when(kk == 0)
        def _():
            acc[...] = jnp.zeros_like(acc)

        acc[...] += _bdot(d_ref[...], w_ref[...])

        @pl.when(kk == NT_IN - 1)
        def _():
            o_ref[...] = acc[...]

    return _pcall(body, name="mm_dh", grid=(tt // tm, NT_IN),
                  in_specs=[pl.BlockSpec((tm, 256), lambda i, kk: (i, kk)),
                            pl.BlockSpec((256, D), lambda i, kk: (_src_block(kk), 0))],
                  out_specs=pl.BlockSpec((tm, D), lambda i, kk: (i, 0)),
                  out_shape=jax.ShapeDtypeStruct((tt, D), F32), scratch=[pltpu.VMEM((tm, D), F32)],
                  carry=carry)(dp, wt)


def _mm_gin(dp, h, tk):
    tt = dp.shape[0]
    nk = tt // tk

    def body(d_ref, h_ref, o_ref, acc):
        kk = pl.program_id(1)

        @pl.when(kk == 0)
        def _():
            acc[...] = jnp.zeros_like(acc)

        acc[...] += _bdot(d_ref[...], h_ref[...], TN)

        @pl.when(kk == nk - 1)
        def _():
            o_ref[...] = acc[...]

    return _pcall(body, name="mm_gin", grid=(NT_IN, nk),
                  in_specs=[pl.BlockSpec((tk, 256), lambda j, kk: (kk, j)),
                            pl.BlockSpec((tk, D), lambda j, kk: (kk, 0))],
                  out_specs=pl.BlockSpec((256, D), lambda j, kk: (_src_block(j), 0)),
                  out_shape=jax.ShapeDtypeStruct((NCOL, D), F32), scratch=[pltpu.VMEM((256, D), F32)])(dp, h)


def _modulate(xin, nw, ss, *, name, sel):
    rows = xin.shape[0]

    def body(x_ref, nw_ref, ss_ref, h_ref):
        x = x_ref[...]
        r = lax.rsqrt(jnp.mean(x * x, axis=-1, keepdims=True) + EPS)
        s = ss_ref[0]
        h_ref[...] = ((x * r * nw_ref[...]) * (1.0 + s[1:2]) + s[0:1]).astype(BF16)

    return _pcall(body, name=name, grid=(rows // TM,),
                  in_specs=[pl.BlockSpec((TM, D), lambda i: (i, 0)), _full((1, D)),
                            pl.BlockSpec((1, 2, D), lambda i: (sel(i), 0, 0))],
                  out_specs=pl.BlockSpec((TM, D), lambda i: (i, 0)),
                  out_shape=jax.ShapeDtypeStruct((rows, D), BF16))(xin, nw, ss)


def _norm_bwd_rows(x, dh, nw, scale):
    r = lax.rsqrt(jnp.mean(x * x, axis=-1, keepdims=True) + EPS)
    xh = x * r
    dxh = dh * ((1.0 + scale) * nw)
    dx = r * (dxh - xh * jnp.mean(dxh * xh, axis=-1, keepdims=True))
    return dx, xh


def _res1_mod2(x, ao, g1, nw2, ss2):
    s_len = x.shape[0]

    def body(x_ref, ao_ref, g_ref, nw_ref, ss_ref, x1_ref, h_ref):
        x1 = x_ref[...] + g_ref[...] * ao_ref[...]
        x1_ref[...] = x1
        r = lax.rsqrt(jnp.mean(x1 * x1, axis=-1, keepdims=True) + EPS)
        s = ss_ref[0]
        h_ref[...] = ((x1 * r * nw_ref[...]) * (1.0 + s[1:2]) + s[0:1]).astype(BF16)

    row = pl.BlockSpec((TM, D), lambda i: (i, 0))
    return _pcall(body, name="res1_mod2", grid=(s_len // TM,),
                  in_specs=[row, row, _full((1, D)), _full((1, D)), _full((1, 2, D))],
                  out_specs=[row, row],
                  out_shape=[jax.ShapeDtypeStruct((s_len, D), F32),
                             jax.ShapeDtypeStruct((s_len, D), BF16)])(x, ao, g1, nw2, ss2)


TS = 1024


def _acc_call(body, *, name, grid, in_specs, out_specs, out_shape, acc_shapes, args):
    return _pcall(body, name=name, grid=grid, in_specs=in_specs, out_specs=out_specs, out_shape=out_shape,
                  scratch=[pltpu.VMEM(s, F32) for s in acc_shapes])(*args)


def _mm_cs(a, w4, *, name):
    m, k = a.shape
    _, _, ns = w4.shape

    def body(a_ref, w_ref, o_ref):
        o_ref[...] = _bdot(a_ref[...], w_ref[0])

    return _pcall(body, name=name, grid=(m // TS, 4),
                  in_specs=[pl.BlockSpec((TS, k), lambda i, j: (i, 0)),
                            pl.BlockSpec((1, k, ns), lambda i, j: (j, 0, 0))],
                  out_specs=pl.BlockSpec((TS, ns), lambda i, j: (i, j)),
                  out_shape=jax.ShapeDtypeStruct((m, 4 * ns), F32))(a, w4)


def _mm_cs_nt(a, w4, *, name):
    m = a.shape[0]
    _, k, ns = w4.shape

    def body(a_ref, w_ref, o_ref, acc):
        j = pl.program_id(1)

        @pl.when(j == 0)
        def _():
            acc[...] = jnp.zeros_like(acc)

        acc[...] += _bdot(a_ref[...], w_ref[0], NT)

        @pl.when(j == 3)
        def _():
            o_ref[...] = acc[...]

    return _acc_call(body, name=name, grid=(m // TS, 4),
                     in_specs=[pl.BlockSpec((TS, ns), lambda i, j: (i, j)),
                               pl.BlockSpec((1, k, ns), lambda i, j: (j, 0, 0))],
                     out_specs=pl.BlockSpec((TS, k), lambda i, j: (i, 0)),
                     out_shape=jax.ShapeDtypeStruct((m, k), F32), acc_shapes=[(TS, k)], args=(a, w4))


def _mm_cs_tn(a, b, ns, *, name):
    s_len, k = a.shape
    nk = s_len // TS

    def body(a_ref, b_ref, o_ref, acc):
        t = pl.program_id(1)

        @pl.when(t == 0)
        def _():
            acc[...] = jnp.zeros_like(acc)

        acc[...] += _bdot(a_ref[...], b_ref[...], TN)

        @pl.when(t == nk - 1)
        def _():
            o_ref[0] = acc[...]

    return _acc_call(body, name=name, grid=(4, nk),
                     in_specs=[pl.BlockSpec((TS, k), lambda j, t: (t, 0)),
                               pl.BlockSpec((TS, ns), lambda j, t: (t, j))],
                     out_specs=pl.BlockSpec((1, k, ns), lambda j, t: (j, 0, 0)),
                     out_shape=jax.ShapeDtypeStruct((4, k, ns), F32), acc_shapes=[(k, ns)], args=(a, b))


def _ffn_up(h2, g4, u4):
    s_len = h2.shape[0]
    ns = g4.shape[1]

    def body(h_ref, g_ref, u_ref, a_ref, b_ref, z_ref):
        h = h_ref[...]
        a = _bdot(h, g_ref[0], NT)
        b = _bdot(h, u_ref[0], NT)
        a_ref[0] = a.astype(BF16)
        b_ref[0] = b.astype(BF16)
        z_ref[0] = (a * _sig(a) * b).astype(BF16)

    w = pl.BlockSpec((1, ns, D), lambda i, j: (j, 0, 0))
    o = pl.BlockSpec((1, TS, ns), lambda i, j: (j, i, 0))
    f = jax.ShapeDtypeStruct((4, s_len, ns), BF16)
    return _pcall(body, name="ffn_up", grid=(s_len // TS, 4),
                  in_specs=[pl.BlockSpec((TS, D), lambda i, j: (i, 0)), w, w], out_specs=[o, o, o],
                  out_shape=[f, f, jax.ShapeDtypeStruct((4, s_len, ns), BF16)])(h2, g4, u4)


def _ffn_down(z4, dn4):
    _, s_len, ns = z4.shape

    def body(z_ref, w_ref, o_ref, acc):
        j = pl.program_id(1)

        @pl.when(j == 0)
        def _():
            acc[...] = jnp.zeros_like(acc)

        acc[...] += _bdot(z_ref[0], w_ref[0])

        @pl.when(j == 3)
        def _():
            o_ref[...] = acc[...]

    return _acc_call(body, name="ffn_down", grid=(s_len // TS, 4),
                     in_specs=[pl.BlockSpec((1, TS, ns), lambda i, j: (j, i, 0)),
                               pl.BlockSpec((1, ns, D), lambda i, j: (j, 0, 0))],
                     out_specs=pl.BlockSpec((TS, D), lambda i, j: (i, 0)),
                     out_shape=jax.ShapeDtypeStruct((s_len, D), F32), acc_shapes=[(TS, D)], args=(z4, dn4))


def _ffn_dz(dyb, dn4, a4, b4):
    _, s_len, ns = a4.shape

    def body(dy_ref, w_ref, a_ref, b_ref, da_ref, db_ref):
        dz = _bdot(dy_ref[...], w_ref[0], NT)
        a = a_ref[0].astype(F32)
        s = _sig(a)
        da_ref[0] = (dz * b_ref[0].astype(F32) * (s * (1.0 + a * (1.0 - s)))).astype(BF16)
        db_ref[0] = (dz * (a * s)).astype(BF16)

    t = pl.BlockSpec((1, TS, ns), lambda i, j: (j, i, 0))
    o = jax.ShapeDtypeStruct((4, s_len, ns), BF16)
    return _pcall(body, name="ffn_dz", grid=(s_len // TS, 4),
                  in_specs=[pl.BlockSpec((TS, D), lambda i, j: (i, 0)),
                            pl.BlockSpec((1, ns, D), lambda i, j: (j, 0, 0)), t, t],
                  out_specs=[t, t], out_shape=[o, o])(dyb, dn4, a4, b4)


def _ffn_gdn(z4, dyb):
    _, s_len, ns = z4.shape
    nk = s_len // TS

    def body(z_ref, dy_ref, o_ref, acc):
        t = pl.program_id(1)

        @pl.when(t == 0)
        def _():
            acc[...] = jnp.zeros_like(acc)

        acc[...] += _bdot(z_ref[0], dy_ref[...], TN)

        @pl.when(t == nk - 1)
        def _():
            o_ref[0] = acc[...]

    return _acc_call(body, name="ffn_gdn", grid=(4, nk),
                     in_specs=[pl.BlockSpec((1, TS, ns), lambda j, t: (j, t, 0)),
                               pl.BlockSpec((TS, D), lambda j, t: (t, 0))],
                     out_specs=pl.BlockSpec((1, ns, D), lambda j, t: (j, 0, 0)),
                     out_shape=jax.ShapeDtypeStruct((4, ns, D), F32), acc_shapes=[(ns, D)], args=(z4, dyb))


def _ffn_dh2(da4, db4, g4, u4):
    _, s_len, ns = da4.shape

    def body(da_ref, db_ref, g_ref, u_ref, o_ref, acc):
        j = pl.program_id(1)

        @pl.when(j == 0)
        def _():
            acc[...] = jnp.zeros_like(acc)

        acc[...] += _bdot(da_ref[0], g_ref[0]) + _bdot(db_ref[0], u_ref[0])

        @pl.when(j == 3)
        def _():
            o_ref[...] = acc[...]

    t = pl.BlockSpec((1, TS, ns), lambda i, j: (j, i, 0))
    w = pl.BlockSpec((1, ns, D), lambda i, j: (j, 0, 0))
    return _acc_call(body, name="ffn_dh2", grid=(s_len // TS, 4), in_specs=[t, t, w, w],
                     out_specs=pl.BlockSpec((TS, D), lambda i, j: (i, 0)),
                     out_shape=jax.ShapeDtypeStruct((s_len, D), F32), acc_shapes=[(TS, D)],
                     args=(da4, db4, g4, u4))


def _ffn_ggu(h2, da4, db4):
    _, s_len, ns = da4.shape
    nk = s_len // TS

    def body(h_ref, da_ref, db_ref, gg_ref, gu_ref, acc_g, acc_u):
        t = pl.program_id(1)

        @pl.when(t == 0)
        def _():
            acc_g[...] = jnp.zeros_like(acc_g)
            acc_u[...] = jnp.zeros_like(acc_u)

        h = h_ref[...]
        acc_g[...] += _bdot(da_ref[0], h, TN)
        acc_u[...] += _bdot(db_ref[0], h, TN)

        @pl.when(t == nk - 1)
        def _():
            gg_ref[0] = acc_g[...]
            gu_ref[0] = acc_u[...]

    d = pl.BlockSpec((1, TS, ns), lambda j, t: (j, t, 0))
    o = pl.BlockSpec((1, ns, D), lambda j, t: (j, 0, 0))
    f = jax.ShapeDtypeStruct((4, ns, D), F32)
    return _acc_call(body, name="ffn_ggu", grid=(4, nk),
                     in_specs=[pl.BlockSpec((TS, D), lambda j, t: (t, 0)), d, d], out_specs=[o, o],
                     out_shape=[f, f], acc_shapes=[(ns, D), (ns, D)], args=(h2, da4, db4))


def _loss_head(x1, y, g2, tgt):
    s_len = x1.shape[0]

    def body(x1_ref, y_ref, g_ref, t_ref, sq_ref, dx2_ref, dyb_ref, dg_ref):
        i = pl.program_id(0)

        @pl.when(i == 0)
        def _():
            sq_ref[...] = jnp.zeros_like(sq_ref)
            dg_ref[...] = jnp.zeros_like(dg_ref)

        y_ = y_ref[...]
        g = g_ref[...]
        e = x1_ref[...] + g * y_ - t_ref[...]
        sq_ref[...] += jnp.sum(e * e, axis=0, keepdims=True)
        dx2 = e * (1.0 / D)
        dx2_ref[...] = dx2
        dyb_ref[...] = (g * dx2).astype(BF16)
        dg_ref[...] += jnp.sum(dx2 * y_, axis=0, keepdims=True)

    row = pl.BlockSpec((TM, D), lambda i: (i, 0))
    vec = _full((1, D))
    return _pcall(body, name="loss_head", grid=(s_len // TM,),
                  in_specs=[row, row, vec, row], out_specs=[vec, row, row, vec],
                  out_shape=[jax.ShapeDtypeStruct((1, D), F32), jax.ShapeDtypeStruct((s_len, D), F32),
                             jax.ShapeDtypeStruct((s_len, D), BF16),
                             jax.ShapeDtypeStruct((1, D), F32)])(x1, y, g2, tgt)


def _mod2_bwd(x1, dh2, dx2, ao, nw2, ss2, g1):
    s_len = x1.shape[0]

    def body(x1_ref, dh_ref, dx2_ref, ao_ref, nw_ref, ss_ref, g_ref,
             dx1_ref, da_ref, dss_ref, dnw_ref, dg_ref):
        i = pl.program_id(0)

        @pl.when(i == 0)
        def _():
            dss_ref[...] = jnp.zeros_like(dss_ref)
            dnw_ref[...] = jnp.zeros_like(dnw_ref)
            dg_ref[...] = jnp.zeros_like(dg_ref)

        dh = dh_ref[...]
        nw = nw_ref[...]
        scale = ss_ref[0][1:2]
        dxn, xh = _norm_bwd_rows(x1_ref[...], dh, nw, scale)
        dx1 = dx2_ref[...] + dxn
        dx1_ref[...] = dx1
        da_ref[...] = (g_ref[...] * dx1).astype(BF16)
        dg_ref[...] += jnp.sum(dx1 * ao_ref[...], axis=0, keepdims=True)
        dsh = jnp.sum(dh, axis=0, keepdims=True)
        dsc = jnp.sum(dh * xh * nw, axis=0, keepdims=True)
        dss_ref[...] += jnp.concatenate([dsh, dsc], axis=0)
        dnw_ref[...] += jnp.sum(dh * xh * (1.0 + scale), axis=0, keepdims=True)

    row = pl.BlockSpec((TM, D), lambda i: (i, 0))
    vec = _full((1, D))
    return _pcall(body, name="mod2_bwd", grid=(s_len // TM,),
                  in_specs=[row, row, row, row, vec, _full((1, 2, D)), vec],
                  out_specs=[row, row, _full((2, D)), vec, vec],
                  out_shape=[jax.ShapeDtypeStruct((s_len, D), F32), jax.ShapeDtypeStruct((s_len, D), BF16),
                             jax.ShapeDtypeStruct((2, D), F32), jax.ShapeDtypeStruct((1, D), F32),
                             jax.ShapeDtypeStruct((1, D), F32)])(x1, dh2, dx2, ao, nw2, ss2, g1)


def _mod1_bwd(tok, dh, dx1, nw1, ss1, carry=None):
    tt = tok.shape[0]
    s_len = dx1.shape[0]

    def body(t_ref, dh_ref, dx1_ref, nw_ref, ss_ref, dx_ref, dss_ref, dnw_ref):
        i = pl.program_id(0)

        @pl.when(i == 0)
        def _():
            dnw_ref[...] = jnp.zeros_like(dnw_ref)

        @pl.when(i <= 1)
        def _():
            dss_ref[...] = jnp.zeros_like(dss_ref)

        dh_ = dh_ref[...]
        nw = nw_ref[...]
        scale = ss_ref[0][1:2]
        dxn, xh = _norm_bwd_rows(t_ref[...], dh_, nw, scale)

        @pl.when(i >= 1)
        def _():
            dx_ref[...] = dx1_ref[...] + dxn

        dsh = jnp.sum(dh_, axis=0, keepdims=True)
        dsc = jnp.sum(dh_ * xh * nw, axis=0, keepdims=True)
        dss_ref[...] += jnp.concatenate([dsh, dsc], axis=0)[None]
        dnw_ref[...] += jnp.sum(dh_ * xh * (1.0 + scale), axis=0, keepdims=True)

    row = pl.BlockSpec((TM, D), lambda i: (i, 0))
    lat = pl.BlockSpec((TM, D), lambda i: (jnp.maximum(i - 1, 0), 0))
    sel = pl.BlockSpec((1, 2, D), lambda i: (jnp.minimum(i, 1), 0, 0))
    return _pcall(body, name="mod1_bwd", grid=(tt // TM,),
                  in_specs=[row, row, lat, _full((1, D)), sel],
                  out_specs=[lat, sel, _full((1, D))],
                  out_shape=[jax.ShapeDtypeStruct((s_len, D), F32), jax.ShapeDtypeStruct((2, 2, D), F32),
                             jax.ShapeDtypeStruct((1, D), F32)], carry=carry)(tok, dh, dx1, nw1, ss1)


def _rows(c):
    return slice(c * CH, (c + 1) * CH)


def _chunk_masks(rev, transpose=False):
    r = lax.broadcasted_iota(jnp.int32, (TM, TM), 0)
    c = lax.broadcasted_iota(jnp.int32, (TM, TM), 1)
    same = (r // CH) == (c // CH)
    before = (c >= r) if (rev != transpose) else (c <= r)
    return same & before, same


def _hgrn_gate(fl, qraw, lg):
    lb = 1.0 / (1.0 + jnp.exp(lg[1:2] - lg[0:1]))
    sg = _sig(fl)
    f = lb + (1.0 - lb) * sg
    q = qraw * _sig(qraw) * (HGD ** -0.5)
    return lb, sg, f, q


def _hgrn_fwd(p, lg, *, rev, carry=None):
    tt = p.shape[0]
    nt = tt // TM
    ncht = TM // CH
    d = 1 if rev else 0

    def tile_of(s):
        return jnp.where(s == 0, 0, nt - s) if rev else s

    def body(f_ref, inp_ref, q_ref, lg_ref, o_ref, st_ref, state):
        s = pl.program_id(0)

        @pl.when(s == 0)
        def _():
            state[...] = jnp.zeros_like(state)

        _, _, f, q = _hgrn_gate(f_ref[...], q_ref[...], lg_ref[0])
        lf = jnp.log(f)
        causal, same = _chunk_masks(rev)
        cum = _dot(causal.astype(F32), lf, prec=HI)
        tot = _dot(same.astype(F32), lf, prec=HI)
        qd = (q * jnp.exp(cum)).astype(BF16)
        kd = ((1.0 - f) * jnp.exp(-cum)).astype(BF16)
        ke = ((1.0 - f) * jnp.exp(tot - cum)).astype(BF16)
        et = jnp.exp(tot)
        v = inp_ref[...].astype(BF16)
        order = range(ncht - 1, -1, -1) if rev else range(ncht)
        outs = []
        for h in range(4):
            sl = slice(h * HGD, (h + 1) * HGD)
            qd_, kd_, ke_, v_ = qd[:, sl], kd[:, sl], ke[:, sl], v[:, sl]
            pm = jnp.where(causal, _dot(qd_, kd_, NT), 0.0).astype(BF16)
            o_h = _dot(pm, v_)
            upd = [_dot(v_[_rows(c)], ke_[_rows(c)], TN) for c in range(ncht)]
            st = state[h]
            for c in order:
                st_ref[c, h] = st
                st = st * et[c * CH:c * CH + 1, sl] + upd[c]
            state[h] = st
            inter = [_dot(qd_[_rows(c)], st_ref[c, h].astype(BF16), NT) for c in range(ncht)]
            outs.append(o_h + jnp.concatenate(inter, axis=0))
        o_ref[...] = jnp.concatenate(outs, axis=1)

    def col(cb):
        return pl.BlockSpec((TM, HGW), lambda s: (tile_of(s), cb))

    return _pcall(
        body, name="hgrn_fwd_rev" if rev else "hgrn_fwd", grid=(nt,),
        in_specs=[col(C_FB if rev else C_FF), col(C_INP), col(C_QHG),
                  pl.BlockSpec((1, 2, HGW), lambda s: (d, 0, 0))],
        out_specs=[pl.BlockSpec((TM, HGW), lambda s: (tile_of(s), 0)),
                   pl.BlockSpec((ncht, 4, HGD, HGD), lambda s: (tile_of(s), 0, 0, 0))],
        out_shape=[jax.ShapeDtypeStruct((tt, HGW), F32),
                   jax.ShapeDtypeStruct((nt * ncht, 4, HGD, HGD), F32)],
        scratch=[pltpu.VMEM((4, HGD, HGD), F32)], carry=carry)(p, p, p, lg)


def _hgrn_bwd(p, lg, do, st, dp, prev, *, rev, carry=None):
    tt = p.shape[0]
    nt = tt // TM
    ncht = TM // CH
    d = 1 if rev else 0
    second = prev is not None

    def tile_of(s):
        return jnp.where(s == nt - 1, 0, s + 1) if rev else nt - 1 - s

    def body(*refs):
        if second:
            (f_ref, inp_ref, q_ref, lg_ref, do_ref, st_ref, dvp_ref, dqp_ref, _dp_in,
             dp_ref, dlg_ref, dstate) = refs
        else:
            (f_ref, inp_ref, q_ref, lg_ref, do_ref, st_ref, _dp_in,
             dp_ref, dv_ref, dq_ref, dlg_ref, dstate) = refs
        s = pl.program_id(0)
        tile = tile_of(s)

        @pl.when(s == 0)
        def _():
            dstate[...] = jnp.zeros_like(dstate)
            dlg_ref[...] = jnp.zeros_like(dlg_ref)

        qraw = q_ref[...]
        lb, sg, f, q = _hgrn_gate(f_ref[...], qraw, lg_ref[0])
        lf = jnp.log(f)
        causal, same = _chunk_masks(rev)
        causal_t, _ = _chunk_masks(rev, transpose=True)
        cum = _dot(causal.astype(F32), lf, prec=HI)
        tot = _dot(same.astype(F32), lf, prec=HI)
        ea, eb, ee, et = jnp.exp(cum), jnp.exp(-cum), jnp.exp(tot - cum), jnp.exp(tot)
        qdf, kdf, kef = q * ea, (1.0 - f) * eb, (1.0 - f) * ee
        qd, kd, ke = qdf.astype(BF16), kdf.astype(BF16), kef.astype(BF16)
        v = inp_ref[...].astype(BF16)
        dob = jnp.where(tile == 0, 0.0, do_ref[...]).astype(BF16)
        order = range(ncht) if rev else range(ncht - 1, -1, -1)
        dq_l, dk_l, dv_l, dcum_l, dtot_l = [], [], [], [], []
        for h in range(4):
            sl = slice(h * HGD, (h + 1) * HGD)
            qd_, kd_, ke_, v_, do_ = qd[:, sl], kd[:, sl], ke[:, sl], v[:, sl], dob[:, sl]
            pmt = jnp.where(causal_t, _dot(kd_, qd_, NT), 0.0).astype(BF16)
            dpm = jnp.where(causal, _dot(do_, v_, NT), 0.0).astype(BF16)
            dpmt = jnp.where(causal_t, _dot(v_, do_, NT), 0.0).astype(BF16)
            dv = _dot(pmt, do_)
            dqd = _dot(dpm, kd_)
            dkd = _dot(dpmt, qd_)
            upd = [_dot(do_[_rows(c)], qd_[_rows(c)], TN) for c in range(ncht)]
            ds = dstate[h]
            ds1 = [None] * ncht
            for c in order:
                ds1[c] = ds
                ds = ds * et[c * CH:c * CH + 1, sl] + upd[c]
            dstate[h] = ds
            dke_c, dv_c, dqd_c, dtot_c = [], [], [], []
            for c in range(ncht):
                st0 = st_ref[c, h]
                dsb = ds1[c].astype(BF16)
                dke_ = _dot(v_[_rows(c)], dsb)
                dke_c.append(dke_)
                dv_c.append(_dot(ke_[_rows(c)], dsb, NT))
                dqd_c.append(_dot(do_[_rows(c)], st0.astype(BF16)))
                dt = (jnp.sum(ds1[c] * st0, axis=0, keepdims=True) * et[c * CH:c * CH + 1, sl]
                      + jnp.sum(dke_ * kef[_rows(c), sl], axis=0, keepdims=True))
                dtot_c.append(jnp.broadcast_to(dt, (CH, HGD)))
            dke = jnp.concatenate(dke_c, axis=0)
            dqd = dqd + jnp.concatenate(dqd_c, axis=0)
            dv_l.append(dv + jnp.concatenate(dv_c, axis=0))
            dtot_l.append(jnp.concatenate(dtot_c, axis=0))
            dq_l.append(dqd * ea[:, sl])
            dk_l.append(dkd * eb[:, sl] + dke * ee[:, sl])
            dcum_l.append(dqd * qdf[:, sl] - dkd * kdf[:, sl] - dke * kef[:, sl])
        dcum = jnp.concatenate(dcum_l, axis=1)
        dlf = _dot(causal_t.astype(F32), dcum, prec=HI) + jnp.concatenate(dtot_l, axis=1)
        dq_t = jnp.concatenate(dq_l, axis=1)
        dv_t = jnp.concatenate(dv_l, axis=1)

        df = dlf / f - jnp.concatenate(dk_l, axis=1)
        dfl = df * (1.0 - lb) * sg * (1.0 - sg)
        dlb = jnp.sum(df * (1.0 - sg), axis=0, keepdims=True)
        dl0 = dlb * lb * (1.0 - lb)
        dlg_ref[...] += jnp.concatenate([dl0, -dl0], axis=0)[None]
        if second:
            sq = _sig(qraw)
            dqr = (dqp_ref[...] + dq_t) * (HGD ** -0.5) * (sq * (1.0 + qraw * (1.0 - sq)))
            dp_ref[...] = jnp.concatenate([dfl, dvp_ref[...] + dv_t, dqr], axis=1).astype(BF16)
        else:
            dp_ref[...] = dfl.astype(BF16)
            dv_ref[...] = dv_t
            dq_ref[...] = dq_t

    def col(cb):
        return pl.BlockSpec((TM, HGW), lambda s: (tile_of(s), cb))

    tok = pl.BlockSpec((TM, HGW), lambda s: (tile_of(s), 0))
    in_specs = [col(C_FB if rev else C_FF), col(C_INP), col(C_QHG),
                pl.BlockSpec((1, 2, HGW), lambda s: (d, 0, 0)),
                pl.BlockSpec((TM, HGW), lambda s: (jnp.maximum(tile_of(s) - 1, 0), 0)),
                pl.BlockSpec((ncht, 4, HGD, HGD), lambda s: (tile_of(s), 0, 0, 0))]
    args = [p, p, p, lg, do, st]
    dlg_spec = _full((1, 2, HGW))
    dlg_shape = jax.ShapeDtypeStruct((1, 2, HGW), F32)
    if second:
        in_specs += [tok, tok]
        args += [prev[0], prev[1]]
        out_specs = [pl.BlockSpec((TM, 3 * HGW), lambda s: (tile_of(s), 0)), dlg_spec]
        out_shape = [jax.ShapeDtypeStruct(dp.shape, BF16), dlg_shape]
    else:
        out_specs = [pl.BlockSpec((TM, HGW), lambda s: (tile_of(s), C_FB if rev else C_FF)), tok, tok, dlg_spec]
        out_shape = [jax.ShapeDtypeStruct(dp.shape, BF16), jax.ShapeDtypeStruct((tt, HGW), F32),
                     jax.ShapeDtypeStruct((tt, HGW), F32), dlg_shape]
    in_specs.append(ANY)
    args.append(dp)
    return _pcall(body, name="hgrn_bwd_rev" if rev else "hgrn_bwd", grid=(nt,),
                  in_specs=in_specs, out_specs=out_specs, out_shape=out_shape,
                  scratch=[pltpu.VMEM((4, HGD, HGD), F32)],
                  aliases={len(args) - 1: 0}, carry=carry)(*args)


def _head_rms(o, w, nheads):
    outs = []
    for h in range(nheads):
        oh = o[:, h * HGD:(h + 1) * HGD]
        outs.append(oh * lax.rsqrt(jnp.mean(oh * oh, axis=-1, keepdims=True) + EPS))
    return jnp.concatenate(outs, axis=1)


def _readout(o0, o1, p, hw4):
    s_len = o0.shape[0] - L

    def body(o0_ref, o1_ref, g_ref, w_ref, y_ref):
        xh = _head_rms(o0_ref[...] + o1_ref[...], None, 4)
        g = g_ref[...]
        y_ref[...] = (xh * w_ref[...] * (g * _sig(g))).astype(BF16)

    lat = pl.BlockSpec((TM, HGW), lambda i: (i + 1, 0))
    return _pcall(body, name="readout", grid=(s_len // TM,),
                  in_specs=[lat, lat, pl.BlockSpec((TM, HGW), lambda i: (i + 1, C_GHG)), _full((1, HGW))],
                  out_specs=pl.BlockSpec((TM, HGW), lambda i: (i, 0)),
                  out_shape=jax.ShapeDtypeStruct((s_len, HGW), BF16))(o0, o1, p, hw4)


def _readout_bwd(o0, o1, p, hw4, dy, dp, carry=None):
    tt = o0.shape[0]
    s_len = tt - L

    def body(o0_ref, o1_ref, g_ref, w_ref, dy_ref, _dp_in, dp_ref, do_ref, dw_ref):
        i = pl.program_id(0)

        @pl.when(i == 0)
        def _():
            dw_ref[...] = jnp.zeros_like(dw_ref)
            dp_ref[...] = jnp.zeros_like(dp_ref)

        @pl.when(i >= 1)
        def _():
            o = o0_ref[...] + o1_ref[...]
            g = g_ref[...]
            w = w_ref[...]
            sg = _sig(g)
            dy_ = dy_ref[...]
            dsw = dy_ * (g * sg)
            outs, xhs = [], []
            for h in range(4):
                sl = slice(h * HGD, (h + 1) * HGD)
                oh = o[:, sl]
                r = lax.rsqrt(jnp.mean(oh * oh, axis=-1, keepdims=True) + EPS)
                xh = oh * r
                dxh = dsw[:, sl] * w[:, sl]
                outs.append(r * (dxh - xh * jnp.mean(dxh * xh, axis=-1, keepdims=True)))
                xhs.append(xh)
            xh = jnp.concatenate(xhs, axis=1)
            do_ref[...] = jnp.concatenate(outs, axis=1)
            dp_ref[...] = (dy_ * xh * w * (sg * (1.0 + g * (1.0 - sg)))).astype(BF16)
            dw_ref[...] += jnp.sum(dsw * xh, axis=0, keepdims=True)

    tok = pl.BlockSpec((TM, HGW), lambda i: (i, 0))
    lat = pl.BlockSpec((TM, HGW), lambda i: (jnp.maximum(i - 1, 0), 0))
    return _pcall(body, name="readout_bwd", grid=(tt // TM,),
                  in_specs=[tok, tok, pl.BlockSpec((TM, HGW), lambda i: (i, C_GHG)), _full((1, HGW)), lat, ANY],
                  out_specs=[pl.BlockSpec((TM, HGW), lambda i: (i, C_GHG)), lat, _full((1, HGW))],
                  out_shape=[jax.ShapeDtypeStruct(dp.shape, BF16), jax.ShapeDtypeStruct((s_len, HGW), F32),
                             jax.ShapeDtypeStruct((1, HGW), F32)],
                  aliases={5: 0}, carry=carry)(o0, o1, p, hw4, dy, dp)


def _rope_tables(s_len):
    t = np.arange(s_len)
    inv = ROPE_THETA ** (-np.arange(0, 32, 2, dtype=np.float64) / 32)
    def half(pos):
        ang = pos[:, None].astype(np.float64) * inv[None, :]
        return (np.concatenate([np.cos(ang), np.cos(ang)], 1), np.concatenate([-np.sin(ang), np.sin(ang)], 1))
    cr, sr = half(t // GRID_W)
    cc, sc = half(t % GRID_W)
    cos = np.concatenate([cr, cc, cr, cc], 1)
    sin = np.concatenate([sr, sc, sr, sc], 1)
    cos = np.concatenate([np.ones((L, 128)), cos], 0)
    sin = np.concatenate([np.zeros((L, 128)), sin], 0)
    return jnp.asarray(cos, F32), jnp.asarray(sin, F32)


def _blockdiag(n, w):
    i = np.arange(n)
    return jnp.asarray((i[:, None] // w == i[None, :] // w) / float(w), F32)


def _dup_matrix():
    m = np.zeros((128, 512), np.float32)
    for g in range(2):
        for j in range(4):
            for dd in range(HDIM):
                m[64 * g + dd, 256 * g + 64 * j + dd] = 1.0
    return m


def _head_mean(x, blockdiag):
    return _dot(x, blockdiag, prec=lax.Precision.HIGH)


def _rot(x):
    n = x.shape[1]
    lane = lax.broadcasted_iota(jnp.int32, x.shape, 1)
    return jnp.where((lane % 32) < 16, pltpu.roll(x, n - 16, 1), pltpu.roll(x, 16, 1))


def _qk_prep(p, cos, sin, qnw8, knw2, bd512, bd128, dup):
    tt = p.shape[0]

    def body(q_ref, kv_ref, cos_ref, sin_ref, qw_ref, kw_ref, b5_ref, b1_ref, dup_ref,
             qr_ref, k4_ref, v4_ref):
        cos_, sin_ = cos_ref[...], sin_ref[...]
        q = q_ref[...]
        qn = q * lax.rsqrt(_head_mean(q * q, b5_ref[...]) + EPS) * qw_ref[...]
        cos4 = jnp.concatenate([cos_] * 4, axis=1)
        sin4 = jnp.concatenate([sin_] * 4, axis=1)
        qr_ref[...] = ((qn * cos4 + _rot(qn) * sin4) * (HDIM ** -0.5)).astype(BF16)
        kv = kv_ref[...]
        k, v = kv[:, :128], kv[:, 128:]
        kn = k * lax.rsqrt(_head_mean(k * k, b1_ref[...]) + EPS) * kw_ref[...]
        kr = kn * cos_ + _rot(kn) * sin_
        k4_ref[...] = _bdot(kr, dup_ref[...]).astype(BF16)
        v4_ref[...] = _bdot(v, dup_ref[...]).astype(BF16)

    row = lambda w, cb: pl.BlockSpec((TM, w), lambda i: (i, cb))
    out = jax.ShapeDtypeStruct((tt, ATW), BF16)
    return _pcall(body, name="qk_prep", grid=(tt // TM,),
                  in_specs=[row(ATW, C_QRAW), row(256, C_KV), row(128, 0), row(128, 0),
                            _full((1, ATW)), _full((1, 128)), _full((ATW, ATW)), _full((128, 128)),
                            _full((128, ATW))],
                  out_specs=[row(ATW, 0)] * 3, out_shape=[out] * 3)(
                      p, p, cos, sin, qnw8, knw2, bd512, bd128, dup)


def _attn_masks(i, nb):
    r = lax.broadcasted_iota(jnp.int32, (4 * BLK, 3 * BLK + L), 0) % BLK
    c = lax.broadcasted_iota(jnp.int32, (4 * BLK, 3 * BLK + L), 1)
    kpos = (i - 1) * BLK + c
    loc = (jnp.abs(c - BLK - r) <= BLK) & (kpos >= 0) & (kpos < nb * BLK)
    return loc | (c >= 3 * BLK)


def _stack_mask():
    r = lax.broadcasted_iota(jnp.int32, (4 * BLK, 256), 0)
    lane = lax.broadcasted_iota(jnp.int32, (4 * BLK, 256), 1)
    return (r // BLK) == (lane // HDIM)


def _stack_heads(xg, fill=0.0):
    x4 = jnp.concatenate([xg] * 4, axis=0)
    return jnp.where(_stack_mask(), x4, jnp.full_like(x4, fill))


def _unstack_heads(x4):
    out = jnp.where(_lane_mask(0), x4[0:BLK], 0.0)
    for j in range(1, 4):
        out = out + jnp.where(_lane_mask(j), x4[j * BLK:(j + 1) * BLK], 0.0)
    return out


def _per_head_rows(vals):
    return jnp.concatenate([jnp.broadcast_to(v, (BLK, 1)) for v in vals], axis=0)


def _lane_mask(j):
    lane = lax.broadcasted_iota(jnp.int32, (1, 256), 1)
    return (lane // HDIM) == j


def _attn_specs(nb):
    blk = lambda off: pl.BlockSpec((BLK, ATW), lambda i: (jnp.clip(i + off, 0, nb - 1) + 2, 0))
    ctx = pl.BlockSpec((L, ATW), lambda i: (0, 0))
    return blk, ctx


def _attn_fwd(qr, k4, v4, sinks, carry=None):
    tt = qr.shape[0]
    s_len = tt - L
    nb = s_len // BLK

    def body(sk_ref, q_ref, kp, ko, kn, kc, vp, vo, vn, vc, y_ref, lse_ref):
        i = pl.program_id(0)
        valid = _attn_masks(i, nb)
        q = q_ref[...]
        ys, lses = [], []
        for g in range(2):
            gs = slice(256 * g, 256 * g + 256)
            kcat = jnp.concatenate([kp[:, gs], ko[:, gs], kn[:, gs], kc[:, gs]], axis=0)
            vcat = jnp.concatenate([vp[:, gs], vo[:, gs], vn[:, gs], vc[:, gs]], axis=0)
            sink = _per_head_rows([sk_ref[4 * g + j] for j in range(4)])
            s = jnp.where(valid, _dot(_stack_heads(q[:, gs]), kcat, NT), -1e30)
            m = jnp.maximum(jnp.max(s, axis=-1, keepdims=True), sink)
            e = jnp.exp(s - m)
            den = jnp.sum(e, axis=-1, keepdims=True) + jnp.exp(sink - m)
            ys.append(_unstack_heads(_bdot(e / den, vcat)))
            lses.append(_unstack_heads(jnp.broadcast_to(m + jnp.log(den), (4 * BLK, 256))))
        y_ref[...] = jnp.concatenate(ys, axis=1).astype(BF16)
        lse_ref[...] = jnp.concatenate(lses, axis=1)

    blk, ctx = _attn_specs(nb)
    out = pl.BlockSpec((BLK, ATW), lambda i: (i, 0))
    return _pcall(body, name="attn_fwd", grid=(nb,),
                  in_specs=[pl.BlockSpec(memory_space=pltpu.SMEM), blk(0),
                            blk(-1), blk(0), blk(1), ctx, blk(-1), blk(0), blk(1), ctx],
                  out_specs=[out, out],
                  out_shape=[jax.ShapeDtypeStruct((s_len, ATW), BF16),
                             jax.ShapeDtypeStruct((s_len, ATW), F32)], carry=carry)(
                      sinks, qr, k4, k4, k4, k4, v4, v4, v4, v4)


def _attn_bwd(qr, k4, v4, sinks, y, lse, dy, carry=None):
    tt = qr.shape[0]
    s_len = tt - L
    nb = s_len // BLK

    def body(sk_ref, q_ref, kp, ko, kn, kc, vp, vo, vn, vc, y_ref, lse_ref, dy_ref,
             dq_ref, dkw_ref, dvw_ref, dkc_ref, dvc_ref, dsk_ref):
        i = pl.program_id(0)

        @pl.when(i == 0)
        def _():
            dkc_ref[...] = jnp.zeros_like(dkc_ref)
            dvc_ref[...] = jnp.zeros_like(dvc_ref)
            dsk_ref[...] = jnp.zeros_like(dsk_ref)

        valid = _attn_masks(i, nb)
        q = q_ref[...]
        dy_ = dy_ref[...]
        dly = dy_ * y_ref[...].astype(F32)
        lse_ = lse_ref[...]
        dqs = []
        for g in range(2):
            gs = slice(256 * g, 256 * g + 256)
            kcat = jnp.concatenate([kp[:, gs], ko[:, gs], kn[:, gs], kc[:, gs]], axis=0)
            vcat = jnp.concatenate([vp[:, gs], vo[:, gs], vn[:, gs], vc[:, gs]], axis=0)
            q4 = _stack_heads(q[:, gs])
            dy4 = _stack_heads(dy_[:, gs]).astype(BF16)
            lse4 = jnp.max(_stack_heads(lse_[:, gs], fill=-1e30), axis=-1, keepdims=True)
            delta = jnp.sum(_stack_heads(dly[:, gs]), axis=-1, keepdims=True)
            sink = _per_head_rows([sk_ref[4 * g + j] for j in range(4)])
            pr = jnp.where(valid, jnp.exp(_dot(q4, kcat, NT) - lse4), 0.0)
            dsb = (pr * (_dot(dy4, vcat, NT) - delta)).astype(BF16)
            dsink = jnp.exp(sink - lse4) * delta
            for j in range(4):
                dsk_ref[4 * g + j:4 * g + j + 1, :] += jnp.broadcast_to(
                    -jnp.sum(dsink[j * BLK:(j + 1) * BLK], axis=0, keepdims=True), (1, 128))
            dqs.append(_unstack_heads(_dot(dsb, kcat)))
            dkg = _dot(dsb, q4, TN)
            dvg = _dot(pr.astype(BF16), dy4, TN)
            dkw_ref[0, :, gs] = dkg[:3 * BLK]
            dvw_ref[0, :, gs] = dvg[:3 * BLK]
            dkc_ref[:, gs] += dkg[3 * BLK:]
            dvc_ref[:, gs] += dvg[3 * BLK:]
        dq_ref[...] = jnp.concatenate(dqs, axis=1)

    blk, ctx = _attn_specs(nb)
    out = pl.BlockSpec((BLK, ATW), lambda i: (i, 0))
    win = pl.BlockSpec((1, 3 * BLK, ATW), lambda i: (i, 0, 0))
    acc = _full((L, ATW))
    return _pcall(body, name="attn_bwd", grid=(nb,),
                  in_specs=[pl.BlockSpec(memory_space=pltpu.SMEM), blk(0),
                            blk(-1), blk(0), blk(1), ctx, blk(-1), blk(0), blk(1), ctx, out, out, out],
                  out_specs=[out, win, win, acc, acc, _full((8, 128))],
                  out_shape=[jax.ShapeDtypeStruct((s_len, ATW), F32),
                             jax.ShapeDtypeStruct((nb, 3 * BLK, ATW), F32),
                             jax.ShapeDtypeStruct((nb, 3 * BLK, ATW), F32),
                             jax.ShapeDtypeStruct((L, ATW), F32), jax.ShapeDtypeStruct((L, ATW), F32),
                             jax.ShapeDtypeStruct((8, 128), F32)], carry=carry)(
                      sinks, qr, k4, k4, k4, k4, v4, v4, v4, v4, y, lse, dy)


def _attn_post(p, cos, sin, qnw8, knw2, bd512, bd128, dupt, dq, dkw, dvw, dkc, dvc, dp, carry=None):
    tt = p.shape[0]
    s_len = tt - L
    nb = s_len // BLK
    nctx = L // BLK

    def body(q_ref, kv_ref, cos_ref, sin_ref, qw_ref, kw_ref, b5_ref, b1_ref, dupt_ref,
             dq_ref, kwp, kwo, kwn, vwp, vwo, vwn, dkc_ref, dvc_ref, _dp_in,
             dp_ref, dqw_ref, dkw_ref):
        t = pl.program_id(0)
        j = t - nctx

        @pl.when(t == 0)
        def _():
            dqw_ref[...] = jnp.zeros_like(dqw_ref)
            dkw_ref[...] = jnp.zeros_like(dkw_ref)

        is_lat = t >= nctx
        cos_, sin_ = cos_ref[...], sin_ref[...]
        has_p = is_lat & (j >= 1)
        has_n = is_lat & (j <= nb - 2)
        dk4 = (jnp.where(is_lat, kwo[0], dkc_ref[...]) + jnp.where(has_p, kwp[0], 0.0)
               + jnp.where(has_n, kwn[0], 0.0))
        dv4 = (jnp.where(is_lat, vwo[0], dvc_ref[...]) + jnp.where(has_p, vwp[0], 0.0)
               + jnp.where(has_n, vwn[0], 0.0))
        dkr = _dot(dk4, dupt_ref[...], prec=HI)
        dv = _dot(dv4, dupt_ref[...], prec=HI)
        kv = kv_ref[...]
        k = kv[:, :128]
        kw = kw_ref[...]
        rk = lax.rsqrt(_head_mean(k * k, b1_ref[...]) + EPS)
        xk = k * rk
        dkn = dkr * cos_ + _rot(dkr * sin_)
        dxk = dkn * kw
        dk = rk * (dxk - xk * _head_mean(dxk * xk, b1_ref[...]))
        dkw_ref[...] += jnp.sum(dkn * xk, axis=0, keepdims=True)
        q = q_ref[...]
        qw = qw_ref[...]
        rq = lax.rsqrt(_head_mean(q * q, b5_ref[...]) + EPS)
        xq = q * rq
        cos4 = jnp.concatenate([cos_] * 4, axis=1)
        sin4 = jnp.concatenate([sin_] * 4, axis=1)
        dqr = jnp.where(is_lat, dq_ref[...], 0.0) * (HDIM ** -0.5)
        dqn = dqr * cos4 + _rot(dqr * sin4)
        dxq = dqn * qw
        dqraw = rq * (dxq - xq * _head_mean(dxq * xq, b5_ref[...]))
        dqw_ref[...] += jnp.sum(dqn * xq, axis=0, keepdims=True)
        dp_ref[...] = jnp.concatenate([dqraw, dk, dv], axis=1).astype(BF16)

    row = lambda w, cb: pl.BlockSpec((BLK, w), lambda t: (t, cb))
    lat = pl.BlockSpec((BLK, ATW), lambda t: (jnp.maximum(t - nctx, 0), 0))

    def part(off):
        return pl.BlockSpec((1, BLK, ATW), lambda t: (jnp.clip(t - nctx + off, 0, nb - 1), 1 - off, 0))

    cacc = pl.BlockSpec((BLK, ATW), lambda t: (jnp.minimum(t, nctx - 1), 0))
    return _pcall(body, name="attn_post", grid=(tt // BLK,),
                  in_specs=[row(ATW, C_QRAW), row(256, C_KV), row(128, 0), row(128, 0),
                            _full((1, ATW)), _full((1, 128)), _full((ATW, ATW)), _full((128, 128)),
                            _full((ATW, 128)), lat, part(-1), part(0), part(1), part(-1), part(0), part(1),
                            cacc, cacc, ANY],
                  out_specs=[pl.BlockSpec((BLK, 768), lambda t: (t, C_QKV)), _full((1, ATW)), _full((1, 128))],
                  out_shape=[jax.ShapeDtypeStruct(dp.shape, BF16), jax.ShapeDtypeStruct((1, ATW), F32),
                             jax.ShapeDtypeStruct((1, 128), F32)],
                  aliases={18: 0}, carry=carry)(p, p, cos, sin, qnw8, knw2, bd512, bd128, dupt,
                                   dq, dkw, dkw, dkw, dvw, dvw, dvw, dkc, dvc, dp)


def _merge(ah, aa, p):
    s_len = ah.shape[0]

    def body(ah_ref, aa_ref, gh_ref, ga_ref, m_ref):
        m_ref[...] = (_sig(gh_ref[...]) * ah_ref[...] + _sig(ga_ref[...]) * aa_ref[...]).astype(BF16)

    row = pl.BlockSpec((TM, D), lambda i: (i, 0))
    return _pcall(body, name="merge", grid=(s_len // TM,),
                  in_specs=[row, row, pl.BlockSpec((TM, D), lambda i: (i + 1, 2)),
                            pl.BlockSpec((TM, D), lambda i: (i + 1, 3))],
                  out_specs=row, out_shape=jax.ShapeDtypeStruct((s_len, D), BF16))(ah, aa, p, p)


def _merge_bwd(dm, ah, aa, p, carry=None):
    tt = p.shape[0]
    s_len = tt - L

    def body(dm_ref, ah_ref, aa_ref, gh_ref, ga_ref, dp_ref, dmh_ref, dma_ref):
        i = pl.program_id(0)

        @pl.when(i == 0)
        def _():
            dp_ref[...] = jnp.zeros_like(dp_ref)

        @pl.when(i >= 1)
        def _():
            dm_ = dm_ref[...]
            sh, sa = _sig(gh_ref[...]), _sig(ga_ref[...])
            dp_ref[...] = jnp.concatenate([dm_ * ah_ref[...] * sh * (1.0 - sh),
                                           dm_ * aa_ref[...] * sa * (1.0 - sa)], axis=1).astype(BF16)
            dmh_ref[...] = (dm_ * sh).astype(BF16)
            dma_ref[...] = (dm_ * sa).astype(BF16)

    lat = pl.BlockSpec((TM, D), lambda i: (jnp.maximum(i - 1, 0), 0))
    return _pcall(body, name="merge_bwd", grid=(tt // TM,),
                  in_specs=[lat, lat, lat, pl.BlockSpec((TM, D), lambda i: (i, 2)),
                            pl.BlockSpec((TM, D), lambda i: (i, 3))],
                  out_specs=[pl.BlockSpec((TM, 2 * D), lambda i: (i, C_GATES)), lat, lat],
                  out_shape=[jax.ShapeDtypeStruct((tt, NCOL), BF16), jax.ShapeDtypeStruct((s_len, D), BF16),
                             jax.ShapeDtypeStruct((s_len, D), BF16)], carry=carry)(dm, ah, aa, p, p)


def _local_step(x, ctx, tgt, mod, modc, nw1, nw2, lg, hw, qnw, knw, sinks,
                w_in, wts, dist=None):
    s_len = x.shape[0]
    tt = s_len + L
    tok = jnp.concatenate([ctx, x], axis=0)
    ss1 = jnp.stack([modc, mod[0:2]])
    ss2 = mod[3:5][None]
    g1, g2 = mod[2:3], mod[5:6]
    hw4 = jnp.tile(hw, (1, 4))
    qnw8 = jnp.tile(qnw, (1, 8))
    knw2 = jnp.tile(knw, (1, 2))
    cos, sin = _rope_tables(s_len)
    bd512, bd128 = _blockdiag(ATW, HDIM), _blockdiag(128, HDIM)
    dupm = _dup_matrix()
    dup, dupt = jnp.asarray(dupm, BF16), jnp.asarray(dupm.T, F32)
    tmt = tt

    def four(b):
        return b.reshape(4, 2 * b.shape[1], b.shape[2])

    def halves(g):
        return g.reshape(4, 2, g.shape[1] // 2, g.shape[2])

    h = _modulate(tok, nw1, ss1, name="mod1", sel=lambda i: jnp.minimum(i, 1))
    if dist is None:
        bh4, ba4, w_o, g4, u4, dn4 = wts
        p = _mm_in(h, w_in, tmt)
        o0, st0 = _hgrn_fwd(p, lg, rev=False)
        o1, st1 = _hgrn_fwd(p, lg, rev=True)
    else:
        core, chip = dist
        half = wts[3].shape[1] // 2
        p, first = _mm_in(h, w_in, tmt, carry=_carry_join(_carry_gather(list(wts[0:3])),
                                                          _carry_gather([wts[3]], rows=[(0, half)])))
        (o0, st0), (g8,) = _hgrn_fwd(p, lg, rev=False, carry=_carry_gather([first[3]], rows=[(half, half)]))
        (o1, st1), (u8,) = _hgrn_fwd(p, lg, rev=True, carry=_carry_gather([wts[4]]))
        bh4, ba4, w_o = four(first[0]), four(first[1]), four(first[2]).reshape(D, D)
        g4, u4 = four(g8), four(u8)
    y_hg = _readout(o0, o1, p, hw4)
    qr, k4, v4 = _qk_prep(p, cos, sin, qnw8, knw2, bd512, bd128, dup)
    if dist is None:
        y_at, lse = _attn_fwd(qr, k4, v4, sinks)
    else:
        (y_at, lse), (dn8,) = _attn_fwd(qr, k4, v4, sinks, carry=_carry_gather([wts[5]]))
        dn4 = four(dn8)
    ah = _mm_cs(y_hg, bh4, name="mm_bh")
    aa = _mm_cs(y_at, ba4, name="mm_ba")
    mixed = _merge(ah, aa, p)
    ao = _mm(mixed, w_o, name="mm_o", tm=512, tn=D, tk=D)
    x1, h2 = _res1_mod2(x, ao, g1, nw2, ss2)
    a4, b4, z4 = _ffn_up(h2, g4, u4)
    y = _ffn_down(z4, dn4)
    sq, dx2, dyb, dg2 = _loss_head(x1, y, g2, tgt)

    da4, db4 = _ffn_dz(dyb, dn4, a4, b4)
    g_dn = _ffn_gdn(z4, dyb)
    dh2 = _ffn_dh2(da4, db4, g4, u4)
    g_g, g_u = _ffn_ggu(h2, da4, db4)
    dx1, dattn, dss2, dnw2, dg1 = _mod2_bwd(x1, dh2, dx2, ao, nw2, ss2, g1)
    dm = _mm(dattn, w_o, name="mm_dm", mode="nt", tm=512, tn=D, tk=D)
    g_o = _mm(mixed, dattn, name="mm_go", mode="tn", tm=D, tn=D, tk=512)
    if dist is None:
        dp, dmh, dma = _merge_bwd(dm, ah, aa, p)
    else:
        ffn_units = [halves(g_dn), halves(g_g), halves(g_u)]
        (dp, dmh, dma), ffn_recv = _merge_bwd(dm, ah, aa, p, carry=_carry_pairx(ffn_units))
        ffn_pairs = _rs_pair_add(ffn_units, ffn_recv, core)
    dy_hg = _mm_cs_nt(dmh, bh4, name="mm_dyh")
    dy_at = _mm_cs_nt(dma, ba4, name="mm_dya")
    g_bh = _mm_cs_tn(y_hg, dmh, D // 4, name="mm_gbh")
    g_ba = _mm_cs_tn(y_at, dma, D // 4, name="mm_gba")
    if dist is None:
        dp, do, dhw4 = _readout_bwd(o0, o1, p, hw4, dy_hg, dp)
        dq, dkw, dvw, dkc, dvc, dsk = _attn_bwd(qr, k4, v4, sinks, y_at, lse, dy_at)
        dp, dqnw8, dknw2 = _attn_post(p, cos, sin, qnw8, knw2, bd512, bd128, dupt, dq, dkw, dvw, dkc, dvc, dp)
    else:
        mix_units = [halves(g_bh), halves(g_ba), halves(g_o.reshape(4, D // 4, D))]
        (dp, do, dhw4), mix_recv = _readout_bwd(o0, o1, p, hw4, dy_hg, dp, carry=_carry_pairx(mix_units))
        mix_pairs = _rs_pair_add(mix_units, mix_recv, core)
        (dq, dkw, dvw, dkc, dvc, dsk), c_dn = _attn_bwd(qr, k4, v4, sinks, y_at, lse, dy_at,
                                                        carry=_carry_chipx(ffn_pairs[0:1]))
        red_dn = _rs_chip_add(ffn_pairs[0:1], c_dn, core, chip)
        (dp, dqnw8, dknw2), post = _attn_post(
            p, cos, sin, qnw8, knw2, bd512, bd128, dupt, dq, dkw, dvw, dkc, dvc, dp,
            carry=_carry_join(_carry_chipx(ffn_pairs[1:2]), _carry_sibx(red_dn)))
        red_g = _rs_chip_add(ffn_pairs[1:2], post[0:1], core, chip)
    if dist is None:
        dp, dv0, dq0, dlg0 = _hgrn_bwd(p, lg, do, st0, dp, None, rev=False)
        dp, dlg1 = _hgrn_bwd(p, lg, do, st1, dp, (dv0, dq0), rev=True)
    else:
        (dp, dv0, dq0, dlg0), mid = _hgrn_bwd(p, lg, do, st0, dp, None, rev=False,
                                              carry=_carry_join(_carry_chipx(ffn_pairs[2:3]), _carry_sibx(red_g)))
        red_u = _rs_chip_add(ffn_pairs[2:3], mid[0:1], core, chip)
        (dp, dlg1), last = _hgrn_bwd(p, lg, do, st1, dp, (dv0, dq0), rev=True,
                                     carry=_carry_join(_carry_chipx(mix_pairs), _carry_sibx(red_u)))
        mix_reds = _rs_chip_add(mix_pairs, last[0:3], core, chip)
        ffn_done = post[1:2] + mid[1:2] + last[3:4]
    if dist is None:
        dh = _mm_dh(dp, w_in, tmt)
    else:
        dh, mix_done = _mm_dh(dp, w_in, tmt, carry=_carry_sibx(mix_reds))
    g_in = _mm_gin(dp, h, tmt)
    if dist is None:
        gx, dss1, dnw1 = _mod1_bwd(tok, dh, dx1, nw1, ss1)
        rs = None
    else:
        in_units = [halves(g_in.reshape(4, NCOL // 4, D))]
        (gx, dss1, dnw1), in_recv = _mod1_bwd(tok, dh, dx1, nw1, ss1, carry=_carry_pairx(in_units))
        rs = dict(ffn_done=ffn_done, mix_done=mix_done, in_units=in_units, in_recv=in_recv)

    dmod = jnp.concatenate([dss1[1], dg1, dss2, dg2], axis=0)
    dmodc = dss1[0]
    raw = (dss1, dg1, dss2, dg2, dnw1, dnw2, dhw4, dqnw8, dknw2, dsk, dlg0, dlg1)
    small = dict(raw=raw, dmod=dmod, dmodc=dmodc, dnw1=dnw1, dnw2=dnw2,
                 dhw=dhw4.reshape(4, HGD).sum(0, keepdims=True),
                 dqnw=dqnw8.reshape(8, HDIM).sum(0, keepdims=True),
                 dknw=dknw2.reshape(2, HDIM).sum(0, keepdims=True),
                 dsinks=dsk[:, 0], dlg=jnp.concatenate([dlg0, dlg1], axis=0))
    big = dict(w_in=g_in, w_bh=g_bh, w_ba=g_ba, w_o=g_o, w_g=g_g, w_u=g_u, w_dn=g_dn)
    return sq, gx, big, small, rs


def _place():
    x, y, c = lax.axis_index("x"), lax.axis_index("y"), lax.axis_index("c")
    return x, y, c


def _gather_blocks(x_refs, out_refs, send_sems, recv_sems, local_sems):
    n = len(out_refs)
    x, y, c = _place()
    me, sibling = (x, y, c), (x, y, 1 - c)
    chips = [(1 - x, y), (x, 1 - y), (1 - x, 1 - y)]

    def slot(u, px, py, pc):
        return out_refs[u].at[4 * px + 2 * py + pc]

    def copy(u, k, block, to, src=None):
        return pltpu.make_async_remote_copy(
            src_ref=slot(u, *block) if src is None else src, dst_ref=slot(u, *block),
            send_sem=send_sems.at[u, k], recv_sem=recv_sems.at[u, k], device_id=to, device_id_type=MESH)

    mines = []
    if x_refs is not None:
        mines = [pltpu.make_async_copy(x_refs[u], slot(u, *me), local_sems.at[u]) for u in range(n)]
    for cp in mines:
        cp.start()
    first = []
    for u in range(n):
        src = None if x_refs is None else x_refs[u]
        first.append(copy(u, 0, me, sibling, src=src))
        first += [copy(u, 1 + j, me, (*chip, c), src=src) for j, chip in enumerate(chips)]
    for cp in first:
        cp.start()
    passed = []
    for j, chip in enumerate(chips):
        for u in range(n):
            copy(u, 1 + j, (*chip, c), me).wait_recv()
            fwd = copy(u, 4 + j, (*chip, c), sibling)
            fwd.start()
            passed.append(fwd)
    for u in range(n):
        copy(u, 0, sibling, me).wait_recv()
    for j, chip in enumerate(chips):
        for u in range(n):
            copy(u, 4 + j, (*chip, 1 - c), me).wait_recv()
    for cp in first + passed:
        cp.wait_send()
    for cp in mines:
        cp.wait()


def _gather_sems(n):
    return [pltpu.SemaphoreType.DMA((n, 7)), pltpu.SemaphoreType.DMA((n, 7)), pltpu.SemaphoreType.DMA((n,))]


def _allgather(blks, *, name, in_vmem):
    n = len(blks)
    space = pltpu.VMEM if in_vmem else pl.ANY

    def body(*refs):
        _gather_blocks(refs[:n], refs[n:2 * n], *refs[2 * n:])

    return pl.pallas_call(
        body, name=name, out_shape=[jax.ShapeDtypeStruct((8,) + b.shape, b.dtype) for b in blks],
        in_specs=[pl.BlockSpec(memory_space=space)] * n, out_specs=[pl.BlockSpec(memory_space=space)] * n,
        scratch_shapes=_gather_sems(n))(*blks)


def _cast_place(ws, c, dev):
    n = len(ws)

    def body(s_ref, *refs):
        for u in range(n):
            refs[n + u][0] = refs[u][...].astype(BF16)

    in_specs, out_specs, out_shape = [], [], []
    for w in ws:
        q, cols = w.shape[0] // 4, w.shape[1]
        in_specs.append(pl.BlockSpec((q, cols), lambda i, s: (2 * s[0] + i, 0)))
        out_specs.append(pl.BlockSpec((1, q, cols), lambda i, s: (s[1], i, 0)))
        out_shape.append(jax.ShapeDtypeStruct((8, 2 * q, cols), BF16))
    return pl.pallas_call(
        body, name="cast_place",
        grid_spec=pltpu.PrefetchScalarGridSpec(num_scalar_prefetch=1, grid=(2,), in_specs=in_specs,
                                               out_specs=out_specs),
        out_shape=out_shape,
        compiler_params=pltpu.CompilerParams(vmem_limit_bytes=48 << 20))(jnp.stack([c, dev]), *ws)


def _gather_phases(out_refs, send_sems, recv_sems, rows=None):
    n = len(out_refs)
    x, y, c = _place()
    me, sibling = (x, y, c), (x, y, 1 - c)
    chips = [(1 - x, y), (x, 1 - y), (1 - x, 1 - y)]

    def copy(u, k, block, to):
        px, py, pc = block
        ref = out_refs[u].at[4 * px + 2 * py + pc]
        if rows is not None and rows[u] is not None:
            ref = ref.at[pl.ds(rows[u][0], rows[u][1])]
        return pltpu.make_async_remote_copy(src_ref=ref, dst_ref=ref, send_sem=send_sems.at[u, k],
                                            recv_sem=recv_sems.at[u, k], device_id=to, device_id_type=MESH)

    def start():
        for u in range(n):
            copy(u, 0, me, sibling).start()
            for j, chip in enumerate(chips):
                copy(u, 1 + j, me, (*chip, c)).start()

    def mid():
        for j, chip in enumerate(chips):
            for u in range(n):
                copy(u, 1 + j, (*chip, c), me).wait_recv()
                copy(u, 4 + j, (*chip, c), sibling).start()

    def end():
        for u in range(n):
            copy(u, 0, sibling, me).wait_recv()
        for j, chip in enumerate(chips):
            for u in range(n):
                copy(u, 4 + j, (*chip, 1 - c), me).wait_recv()
        for u in range(n):
            copy(u, 0, me, sibling).wait_send()
            for j, chip in enumerate(chips):
                copy(u, 1 + j, me, (*chip, c)).wait_send()
                copy(u, 4 + j, (*chip, c), sibling).wait_send()

    return start, mid, end


def _carry_gather(bufs, rows=None):
    n = len(bufs)
    return _Carry(bufs, [jax.ShapeDtypeStruct(b.shape, b.dtype) for b in bufs], {u: u for u in range(n)},
                  [pltpu.SemaphoreType.DMA((n, 7)), pltpu.SemaphoreType.DMA((n, 7))],
                  lambda ins, outs, sems: _gather_phases(outs, *sems, rows=rows))


def _allgather_inplace(bufs, *, name):
    n = len(bufs)

    def body(*refs):
        for phase in _gather_phases(refs[n:2 * n], *refs[2 * n:]):
            phase()

    return pl.pallas_call(
        body, name=name, out_shape=[jax.ShapeDtypeStruct(b.shape, b.dtype) for b in bufs],
        in_specs=[ANY] * n, out_specs=[ANY] * n, input_output_aliases={u: u for u in range(n)},
        scratch_shapes=[pltpu.SemaphoreType.DMA((n, 7)), pltpu.SemaphoreType.DMA((n, 7))])(*bufs)


def _ag_small(raw):
    def body(dss1, dg1, dss2, dg2, dnw1, dnw2, dhw4, dqnw8, dknw2, dsk, dlg0, dlg1,
             out_ref, tot_ref, blk, send_sems, recv_sems, local_sems):
        blk[...] = jnp.zeros_like(blk)
        blk[0:2, :] = dss1[1]
        blk[2:3, :] = dg1[...]
        blk[3:5, :] = dss2[...]
        blk[5:6, :] = dg2[...]
        blk[6:8, :] = dss1[0]
        blk[8:9, :] = dnw1[...]
        blk[9:10, :] = dnw2[...]
        blk[10:11, 0:HGW] = dhw4[...]
        blk[10:11, HGW:D] = dqnw8[...]
        blk[11:12, 0:128] = dknw2[...]
        blk[12:14, 0:HGW] = dlg0[0]
        blk[14:16, 0:HGW] = dlg1[0]
        blk[16:24, 0:128] = dsk[...]
        _gather_blocks([blk], [out_ref], send_sems, recv_sems, local_sems)
        acc = out_ref[0]
        for i in range(1, 8):
            acc = acc + out_ref[i]
        tot_ref[...] = acc

    vm = pl.BlockSpec(memory_space=pltpu.VMEM)
    return pl.pallas_call(
        body, name="ag_small",
        out_shape=[jax.ShapeDtypeStruct((8, 24, D), F32), jax.ShapeDtypeStruct((24, D), F32)],
        in_specs=[vm] * 12, out_specs=[vm, vm],
        scratch_shapes=[pltpu.VMEM((24, D), F32)] + _gather_sems(1))(*raw)


def _rs_pair_exchange(units):
    n = len(units)

    def body(*refs):
        start, _, end = _pairx_phases(refs[:n], refs[n:2 * n], *refs[2 * n:])
        start()
        end()

    return pl.pallas_call(
        body, name="rs_pair_exchange", out_shape=_pairx_shapes(units),
        in_specs=[ANY] * n, out_specs=[ANY] * n,
        scratch_shapes=[pltpu.SemaphoreType.DMA((n, 4)), pltpu.SemaphoreType.DMA((n, 4))])(*units)


def _pairx_shapes(units):
    return [jax.ShapeDtypeStruct((4,) + g.shape[2:], g.dtype) for g in units]


def _pairx_phases(g_refs, r_refs, send_sems, recv_sems):
    n = len(g_refs)
    x, y, c = _place()
    cps = [pltpu.make_async_remote_copy(
        src_ref=g_refs[u].at[j, 1 - c], dst_ref=r_refs[u].at[j], send_sem=send_sems.at[u, j],
        recv_sem=recv_sems.at[u, j], device_id=(x, y, 1 - c), device_id_type=MESH)
        for u in range(n) for j in range(4)]

    def start():
        for cp in cps:
            cp.start()

    def end():
        for cp in cps:
            cp.wait()

    return start, None, end


def _carry_pairx(units):
    n = len(units)
    return _Carry(units, _pairx_shapes(units), {},
                  [pltpu.SemaphoreType.DMA((n, 4)), pltpu.SemaphoreType.DMA((n, 4))],
                  lambda ins, outs, sems: _pairx_phases(ins, outs, *sems))


def _rs_pair_add(units, recvs, c):
    n = len(units)

    def body(c_ref, *refs):
        for u in range(n):
            refs[2 * n + u][...] = (refs[u][0] + refs[n + u][...]).astype(BF16)

    in_specs, out_specs, out_shape = [], [], []
    for g in units:
        h, w = g.shape[2] // 2, g.shape[3]
        in_specs.append(pl.BlockSpec((1, 1, h, w), lambda j, i, cr: (j, cr[0], i, 0)))
    for g in units:
        h, w = g.shape[2] // 2, g.shape[3]
        in_specs.append(pl.BlockSpec((1, h, w), lambda j, i, cr: (j, i, 0)))
        out_specs.append(pl.BlockSpec((1, h, w), lambda j, i, cr: (j, i, 0)))
        out_shape.append(jax.ShapeDtypeStruct((4, 2 * h, w), BF16))
    return pl.pallas_call(
        body, name="rs_pair_add",
        grid_spec=pltpu.PrefetchScalarGridSpec(num_scalar_prefetch=1, grid=(4, 2), in_specs=in_specs,
                                               out_specs=out_specs),
        out_shape=out_shape,
        compiler_params=pltpu.CompilerParams(vmem_limit_bytes=48 << 20))(c.reshape(1), *units, *recvs)


def _rs_chip_exchange(pairs):
    n = len(pairs)

    def body(*refs):
        start, _, end = _chipx_phases(refs[:n], refs[n:2 * n], *refs[2 * n:])
        start()
        end()

    return pl.pallas_call(
        body, name="rs_chip_exchange", out_shape=[jax.ShapeDtypeStruct(p.shape, p.dtype) for p in pairs],
        in_specs=[ANY] * n, out_specs=[ANY] * n,
        scratch_shapes=[pltpu.SemaphoreType.DMA((n, 3)), pltpu.SemaphoreType.DMA((n, 3))])(*pairs)


def _chipx_phases(p_refs, r_refs, send_sems, recv_sems):
    n = len(p_refs)
    x, y, c = _place()
    k = 2 * x + y
    sends = []
    for d in range(1, 4):
        j = (k + d) % 4
        for u in range(n):
            sends.append(pltpu.make_async_remote_copy(
                src_ref=p_refs[u].at[j], dst_ref=r_refs[u].at[k], send_sem=send_sems.at[u, d - 1],
                recv_sem=recv_sems.at[u, d - 1], device_id=(j // 2, j % 2, c), device_id_type=MESH))

    def start():
        for cp in sends:
            cp.start()

    def end():
        for d in range(1, 4):
            src = (k + 4 - d) % 4
            for u in range(n):
                pltpu.make_async_remote_copy(
                    src_ref=p_refs[u].at[src], dst_ref=r_refs[u].at[src], send_sem=send_sems.at[u, d - 1],
                    recv_sem=recv_sems.at[u, d - 1], device_id=(x, y, c), device_id_type=MESH).wait_recv()
        for cp in sends:
            cp.wait_send()

    return start, None, end


def _carry_chipx(pairs):
    n = len(pairs)
    return _Carry(pairs, [jax.ShapeDtypeStruct(p.shape, p.dtype) for p in pairs], {},
                  [pltpu.SemaphoreType.DMA((n, 3)), pltpu.SemaphoreType.DMA((n, 3))],
                  lambda ins, outs, sems: _chipx_phases(ins, outs, *sems))


def _rs_chip_add(pairs, contribs, c, chip):
    n = len(pairs)

    def body(s_ref, *refs):
        for u in range(n):
            a, b, c_, d = refs[4 * u:4 * u + 4]
            refs[4 * n + u][0] = ((a[0].astype(F32) + b[0].astype(F32)) + c_[0].astype(F32)) + d[0].astype(F32)

    in_specs, out_specs, out_shape, args = [], [], [], []
    for p, r in zip(pairs, contribs):
        h, w = p.shape[1] // 2, p.shape[2]
        in_specs += [pl.BlockSpec((1, h, w), functools.partial(lambda d, i, s: ((s[1] + d) % 4, i, 0), d))
                     for d in range(4)]
        args += [p, r, r, r]
        out_specs.append(pl.BlockSpec((1, h, w), lambda i, s: (s[0], i, 0)))
        out_shape.append(jax.ShapeDtypeStruct((2, 2 * h, w), F32))
    return pl.pallas_call(
        body, name="rs_chip_add",
        grid_spec=pltpu.PrefetchScalarGridSpec(num_scalar_prefetch=1, grid=(2,), in_specs=in_specs,
                                               out_specs=out_specs),
        out_shape=out_shape,
        compiler_params=pltpu.CompilerParams(vmem_limit_bytes=48 << 20))(jnp.stack([c, chip]), *args)


def _rs_sibling_gather(reds):
    n = len(reds)

    def body(*refs):
        start, _, end = _sibx_phases(refs[n:2 * n], *refs[2 * n:])
        start()
        end()

    return pl.pallas_call(
        body, name="rs_sibling_gather", out_shape=[jax.ShapeDtypeStruct(r.shape, r.dtype) for r in reds],
        in_specs=[ANY] * n, out_specs=[ANY] * n, input_output_aliases={u: u for u in range(n)},
        scratch_shapes=[pltpu.SemaphoreType.DMA((n,))] * 2)(*reds)


def _sibx_phases(o_refs, send_sems, recv_sems):
    n = len(o_refs)
    x, y, c = _place()
    cps = [pltpu.make_async_remote_copy(
        src_ref=o_refs[u].at[c], dst_ref=o_refs[u].at[c], send_sem=send_sems.at[u], recv_sem=recv_sems.at[u],
        device_id=(x, y, 1 - c), device_id_type=MESH) for u in range(n)]

    def start():
        for cp in cps:
            cp.start()

    def end():
        for u in range(n):
            cps[u].wait_send()
            pltpu.make_async_remote_copy(
                src_ref=o_refs[u].at[1 - c], dst_ref=o_refs[u].at[1 - c], send_sem=send_sems.at[u],
                recv_sem=recv_sems.at[u], device_id=(x, y, 1 - c), device_id_type=MESH).wait_recv()

    return start, None, end


def _carry_sibx(reds):
    n = len(reds)
    return _Carry(reds, [jax.ShapeDtypeStruct(r.shape, r.dtype) for r in reds], {u: u for u in range(n)},
                  [pltpu.SemaphoreType.DMA((n,))] * 2, lambda ins, outs, sems: _sibx_phases(outs, *sems))


def _prologue(blk, c_ctx, w, b, in8):
    n = w.shape[1]

    def body(blk_ref, cctx_ref, w_ref, b_ref, _in_in, g0_ref, c16_ref, g1_ref, in_ref, mod_s,
             s1, r1, l1, s2, r2, l2, s3, r3):
        start, mid, end = _gather_phases([in_ref], s3, r3)
        start()
        _gather_blocks([blk_ref], [g0_ref], s1, r1, l1)
        c16 = jnp.concatenate([g0_ref[i, 0:1, :] for i in range(8)] + [cctx_ref[...], jnp.zeros((7, D), F32)],
                              axis=0)
        c16_ref[...] = c16
        mod_s[...] = _dot(c16 * _sig(c16), w_ref[...], prec=HI) + b_ref[...]
        _gather_blocks([mod_s], [g1_ref], s2, r2, l2)
        mid()
        end()

    vm = pl.BlockSpec(memory_space=pltpu.VMEM)
    return pl.pallas_call(
        body, name="prologue",
        out_shape=[jax.ShapeDtypeStruct((8, 8, D), F32), jax.ShapeDtypeStruct((16, D), F32),
                   jax.ShapeDtypeStruct((8, 16, n), F32), jax.ShapeDtypeStruct(in8.shape, in8.dtype)],
        in_specs=[vm, vm, vm, vm, ANY], out_specs=[vm, vm, vm, ANY], input_output_aliases={4: 3},
        scratch_shapes=[pltpu.VMEM((16, n), F32)] + _gather_sems(1) + _gather_sems(1)
        + [pltpu.SemaphoreType.DMA((1, 7)), pltpu.SemaphoreType.DMA((1, 7))],
        compiler_params=pltpu.CompilerParams(vmem_limit_bytes=48 << 20))(blk, c_ctx, w, b, in8)


def _ada_bwd(c16, dmod16, w):
    n = w.shape[1]
    tn = 512

    def body(c_ref, d_ref, w_ref, gw_ref, gc_ref):
        j = pl.program_id(0)

        @pl.when(j == 0)
        def _():
            gc_ref[...] = jnp.zeros_like(gc_ref)

        cc = c_ref[...]
        dm = d_ref[...]
        gw_ref[...] = _dot(cc * _sig(cc), dm, TN, prec=HI)
        gc_ref[...] += _dot(dm, w_ref[...], NT, prec=HI)

    return _pcall(body, name="ada_bwd", grid=(n // tn,),
                  in_specs=[_full((16, D)), pl.BlockSpec((16, tn), lambda j: (0, j)),
                            pl.BlockSpec((D, tn), lambda j: (0, j))],
                  out_specs=[pl.BlockSpec((D, tn), lambda j: (0, j)), _full((16, D))],
                  out_shape=[jax.ShapeDtypeStruct((D, n), F32),
                             jax.ShapeDtypeStruct((16, D), F32)])(c16, dmod16, w)


def _adam_math(w, g, m, v):
    c1 = 1.0 - ADAM_B1 ** ADAM_STEP
    c2 = 1.0 - ADAM_B2 ** ADAM_STEP
    nm = ADAM_B1 * m + (1.0 - ADAM_B1) * g
    nv = ADAM_B2 * v + (1.0 - ADAM_B2) * (g * g)
    return -ADAM_LR * ((nm / c1) / (jnp.sqrt(nv / c2) + ADAM_EPS) + ADAM_WD * w), nm, nv


def _adamw_small(ws, gs, ms, vs):
    n = len(ws)

    def body(*refs):
        for u in range(n):
            d_, nm, nv = _adam_math(refs[u][...], refs[n + u][...], refs[2 * n + u][...], refs[3 * n + u][...])
            refs[4 * n + u][...] = d_
            refs[5 * n + u][...] = nm
            refs[6 * n + u][...] = nv

    specs = [_full(w.shape) for w in ws]
    shapes = [jax.ShapeDtypeStruct(w.shape, F32) for w in ws]
    out = _pcall(body, name="adamw_small", grid=(1,), in_specs=specs * 4, out_specs=specs * 3,
                 out_shape=shapes * 3)(*ws, *gs, *ms, *vs)
    return out[:n], out[n:2 * n], out[2 * n:]


def _cctx_grad(parts, c_ctx):
    def body(p_ref, c_ref, o_ref):
        acc = p_ref[0:1, :]
        for k in range(1, 4):
            acc = acc + p_ref[k:k + 1, :]
        cc = c_ref[...]
        s = _sig(cc)
        o_ref[...] = acc * (s * (1.0 + cc * (1.0 - s)))

    return _pcall(body, name="cctx_grad", grid=(1,), in_specs=[_full(parts.shape), _full((1, D))],
                  out_specs=_full((1, D)), out_shape=jax.ShapeDtypeStruct((1, D), F32))(parts, c_ctx)


ADAM_STEPS = 8


def _adamw_multi(ws, gs, ms, vs, *, name, carry=None):
    n = len(ws)

    def body(*refs):
        for u in range(n):
            refs[4 * n + u][...], refs[5 * n + u][...], refs[6 * n + u][...] = _adam_math(
                refs[u][...], refs[n + u][...], refs[2 * n + u][...], refs[3 * n + u][...])

    specs = [pl.BlockSpec((w.shape[0] // ADAM_STEPS, w.shape[1]), lambda i: (i, 0)) for w in ws]
    shapes = [jax.ShapeDtypeStruct(w.shape, F32) for w in ws]
    res = _pcall(body, name=name, grid=(ADAM_STEPS,), in_specs=specs * 4, out_specs=specs * 3,
                 out_shape=shapes * 3, carry=carry)(*ws, *gs, *ms, *vs)
    out, extra = res if carry is not None else (res, None)
    return (out[:n], out[n:2 * n], out[2 * n:]), extra


def kernel(x, c, ctx, c_ctx, w_ada, b_ada, norm_mix_w, norm_ffn_w, w_in, hgrn_lb_logits, hgrn_norm_w, q_norm_w, k_norm_w, attn_sinks, w_branch_hgrn, w_branch_attn, w_out, w_ffn_gate, w_ffn_up, w_ffn_down, loss_target, m_c_ctx, m_w_ada, m_b_ada, m_norm_mix_w, m_norm_ffn_w, m_w_in, m_hgrn_lb_logits, m_hgrn_norm_w, m_q_norm_w, m_k_norm_w, m_attn_sinks, m_w_branch_hgrn, m_w_branch_attn, m_w_out, m_w_ffn_gate, m_w_ffn_up, m_w_ffn_down, v_c_ctx, v_w_ada, v_b_ada, v_norm_mix_w, v_norm_ffn_w, v_w_in, v_hgrn_lb_logits, v_hgrn_norm_w, v_q_norm_w, v_k_norm_w, v_attn_sinks, v_w_branch_hgrn, v_w_branch_attn, v_w_out, v_w_ffn_gate, v_w_ffn_up, v_w_ffn_down):
    xi, yi, ci = _place()
    chip = 2 * xi + yi
    dev = 2 * chip + ci
    s_len = x.shape[1]

    shards = [w_in[0].T, w_branch_hgrn[0], w_branch_attn[0], w_out[0], w_ffn_gate[0].T, w_ffn_up[0].T,
              w_ffn_down[0]]
    bufs = _cast_place(shards, ci, dev)

    lbrow = jnp.pad(hgrn_lb_logits.reshape(1, 512), ((0, 0), (0, D - 512)))
    blk = jnp.concatenate([c, lbrow, jnp.zeros((6, D), F32)], axis=0)
    nada = w_ada.shape[2]
    b_sh = lax.dynamic_slice(b_ada, (0, chip * nada), (1, nada))
    g0, c16, g1, in8 = _prologue(blk, c_ctx[None], w_ada[0], b_sh, bufs[0])
    lg = g0[0::2, 1, :512].reshape(4, 2, 2, 128).transpose(1, 2, 0, 3).reshape(2, 2, HGW)
    modall = g1[0::2].transpose(1, 0, 2).reshape(16, 4 * nada)
    mod = lax.dynamic_slice(modall, (dev, 0), (1, 6 * D)).reshape(6, D)
    modc = modall[8].reshape(6, D)[:2]

    sq, gx, _, small, rs = _local_step(
        x[0], ctx[0], loss_target[0], mod, modc, norm_mix_w, norm_ffn_w, lg, hgrn_norm_w, q_norm_w,
        k_norm_w, attn_sinks[0], in8.reshape(NCOL, D), bufs[1:], dist=(ci, chip))
    loss = lax.psum(0.5 * jnp.sum(sq) / D, ("x", "y", "c"))

    def whole(r):
        return r.reshape(2 * r.shape[1], r.shape[2])

    g_dn, g_g, g_u = [whole(r) for r in rs["ffn_done"]]
    g_bh, g_ba, g_o = [whole(r) for r in rs["mix_done"]]
    in_pairs = _rs_pair_add(rs["in_units"], rs["in_recv"], ci)

    g2, tot = _ag_small(small["raw"])
    dmodc_tot = jnp.pad(tot[6:8].reshape(1, 2 * D), ((0, 0), (0, 4 * D)))
    g_b_ada = tot[0:6].reshape(1, 6 * D) + dmodc_tot
    dmod16 = jnp.concatenate([g2[:, 0:6].reshape(8, 6 * D), dmodc_tot, jnp.zeros((7, 6 * D), F32)], axis=0)
    g_w_ada, gc_part = _ada_bwd(c16, lax.dynamic_slice(dmod16, (0, chip * nada), (16, nada)), w_ada[0])
    g3, = _allgather([gc_part[8:16]], name="ag_cctx", in_vmem=True)
    g_c_ctx = _cctx_grad(g3[0::2, 0], c_ctx[None])[0]
    g_nw1 = tot[8:9]
    g_nw2 = tot[9:10]
    g_hw = tot[10, :HGW].reshape(4, HGD).sum(0, keepdims=True)
    g_qnw = tot[10, HGW:].reshape(8, HDIM).sum(0, keepdims=True)
    g_knw = tot[11, :128].reshape(2, HDIM).sum(0, keepdims=True)
    g_sinks = tot[16:24, 0][None]
    g_lg = lax.dynamic_slice(tot[12:16, :HGW].reshape(2, 2, HGW), (0, 0, chip * 128), (2, 2, 128))

    names = ["c_ctx", "w_ada", "b_ada", "norm_mix_w", "norm_ffn_w", "w_in", "hgrn_lb_logits", "hgrn_norm_w",
             "q_norm_w", "k_norm_w", "attn_sinks", "w_branch_hgrn", "w_branch_attn", "w_out", "w_ffn_gate",
             "w_ffn_up", "w_ffn_down"]
    ws = dict(zip(names, [c_ctx, w_ada, b_ada, norm_mix_w, norm_ffn_w, w_in, hgrn_lb_logits, hgrn_norm_w,
                          q_norm_w, k_norm_w, attn_sinks, w_branch_hgrn, w_branch_attn, w_out, w_ffn_gate,
                          w_ffn_up, w_ffn_down]))
    ms = dict(zip(names, [m_c_ctx, m_w_ada, m_b_ada, m_norm_mix_w, m_norm_ffn_w, m_w_in, m_hgrn_lb_logits,
                          m_hgrn_norm_w, m_q_norm_w, m_k_norm_w, m_attn_sinks, m_w_branch_hgrn,
                          m_w_branch_attn, m_w_out, m_w_ffn_gate, m_w_ffn_up, m_w_ffn_down]))
    vs = dict(zip(names, [v_c_ctx, v_w_ada, v_b_ada, v_norm_mix_w, v_norm_ffn_w, v_w_in, v_hgrn_lb_logits,
                          v_hgrn_norm_w, v_q_norm_w, v_k_norm_w, v_attn_sinks, v_w_branch_hgrn,
                          v_w_branch_attn, v_w_out, v_w_ffn_gate, v_w_ffn_up, v_w_ffn_down]))
    transposed = ("w_in", "w_ffn_gate", "w_ffn_up")

    def view(a, n):
        return a[0].T if n in transposed else a[0]

    def unview(a, n):
        return a.T[None] if n in transposed else a[None]

    delta, new_m, new_v, grads = {}, {}, {}, {}

    def big_adamw(group, gs, name, carry=None):
        (d_, m_, v_), extra = _adamw_multi([view(ws[n], n) for n in group], gs, [view(ms[n], n) for n in group],
                                           [view(vs[n], n) for n in group], name=name, carry=carry)
        for i, n in enumerate(group):
            grads[n], delta[n], new_m[n], new_v[n] = (unview(gs[i], n), unview(d_[i], n), unview(m_[i], n),
                                                      unview(v_[i], n))
        return extra

    in_contribs = big_adamw(["w_ffn_down", "w_ffn_gate", "w_ffn_up", "w_out", "w_branch_hgrn", "w_branch_attn"],
                            [g_dn, g_g, g_u, g_o, g_bh, g_ba], "adamw_first", carry=_carry_chipx(in_pairs))
    in_reds = _rs_chip_add(in_pairs, in_contribs, ci, chip)
    g_in, = [whole(r) for r in _rs_sibling_gather(in_reds)]
    big_adamw(["w_in", "w_ada"], [g_in, g_w_ada], "adamw_second")
    grads.update(c_ctx=g_c_ctx, b_ada=g_b_ada, norm_mix_w=g_nw1, norm_ffn_w=g_nw2, hgrn_lb_logits=g_lg,
                 hgrn_norm_w=g_hw, q_norm_w=g_qnw, k_norm_w=g_knw, attn_sinks=g_sinks)
    small_names = [n for n in names if n not in delta]

    def two_d(a):
        return a.reshape(1, -1) if a.ndim == 1 else a

    sd, sm_, sv = _adamw_small(*[[two_d(d[n]) for n in small_names] for d in (ws, grads, ms, vs)])
    for i, n in enumerate(small_names):
        for dst, src in ((delta, sd), (new_m, sm_), (new_v, sv)):
            dst[n] = src[i].reshape(ws[n].shape)
    return (loss, gx[None], *[grads[n] for n in names], *[delta[n] for n in names],
            *[new_m[n] for n in names], *[new_v[n] for n in names])
```

```python
import functools

import numpy as np
import jax
import jax.numpy as jnp
from jax import lax
from jax.experimental import pallas as pl
from jax.experimental.pallas import tpu as pltpu

F32 = jnp.float32
BF16 = jnp.bfloat16
HI = lax.Precision.HIGHEST
MESH = pl.DeviceIdType.MESH

D = 1024
L = 256
TM = 256
HGW = 512
HGD = 128
CH = 32
ATW = 512
HDIM = 64
BLK = 128
GRID_W = 64
DFF = 2816
NCOL = 5376
EPS = 1e-6
ROPE_THETA = 10000.0

C_FB, C_INP, C_QHG, C_FF = 0, 1, 2, 3
C_GATES = 1
C_GHG, C_QRAW = 8, 9
C_KV = 20
C_QKV = 6

ADAM_LR, ADAM_B1, ADAM_B2, ADAM_EPS, ADAM_WD, ADAM_STEP = 0.001, 0.9, 0.999, 1e-08, 0.01, 10

NN = (((1,), (0,)), ((), ()))
NT = (((1,), (1,)), ((), ()))
TN = (((0,), (0,)), ((), ()))


def _dot(a, b, dims=NN, prec=None):
    return lax.dot_general(a, b, dims, precision=prec, preferred_element_type=F32)


def _bdot(a, b, dims=NN):
    return _dot(a.astype(BF16), b.astype(BF16), dims)


def _sig(x):
    return 1.0 / (1.0 + jnp.exp(-x))


class _Carry:
    def __init__(self, ins, outs, aliases, scratch, phases):
        self.ins, self.outs, self.aliases, self.scratch, self.phases = ins, outs, aliases, scratch, phases


def _in_hbm(args):
    return [pltpu.with_memory_space_constraint(a, pltpu.HBM) for a in args]


def _carry_join(a, b):
    na_in, na_out, na_sc = len(a.ins), len(a.outs), len(a.scratch)
    aliases = dict(a.aliases)
    aliases.update({na_in + i: na_out + o for i, o in b.aliases.items()})

    def phases(ins, outs, sems):
        pa = a.phases(ins[:na_in], outs[:na_out], sems[:na_sc])
        pb = b.phases(ins[na_in:], outs[na_out:], sems[na_sc:])

        def both(fa, fb):
            if fa is None and fb is None:
                return None

            def run():
                for fn in (fa, fb):
                    if fn is not None:
                        fn()
            return run

        return tuple(both(fa, fb) for fa, fb in zip(pa, pb))

    return _Carry(list(a.ins) + list(b.ins), list(a.outs) + list(b.outs), aliases,
                  list(a.scratch) + list(b.scratch), phases)


def _pcall(body, *, name, grid, in_specs, out_specs, out_shape, scratch=(), aliases=None, vmem_mb=48,
           carry=None):
    params = pltpu.CompilerParams(dimension_semantics=("arbitrary",) * len(grid),
                                  vmem_limit_bytes=vmem_mb << 20)
    if carry is None:
        plain = pl.pallas_call(
            body, name=name, grid=grid, in_specs=in_specs, out_specs=out_specs, out_shape=out_shape,
            scratch_shapes=list(scratch), input_output_aliases=aliases or {}, compiler_params=params)
        return lambda *args: plain(*_in_hbm(args))
    single = not isinstance(out_shape, (list, tuple))
    out_specs_l = [out_specs] if single else list(out_specs)
    out_shape_l = [out_shape] if single else list(out_shape)
    n_in, n_out, n_sc = len(in_specs), len(out_shape_l), len(scratch)
    k_in, k_out = len(carry.ins), len(carry.outs)
    nsteps = int(np.prod(grid))
    assert nsteps >= 3

    def wrapped(*refs):
        ins, cins = refs[:n_in], refs[n_in:n_in + k_in]
        o0 = n_in + k_in
        outs, couts = refs[o0:o0 + n_out], refs[o0 + n_out:o0 + n_out + k_out]
        s0 = o0 + n_out + k_out
        sc, csc = refs[s0:s0 + n_sc], refs[s0 + n_sc:]
        step = pl.program_id(0)
        for ax in range(1, len(grid)):
            step = step * grid[ax] + pl.program_id(ax)
        start, mid, end = carry.phases(cins, couts, csc)
        pl.when(step == 0)(start)
        body(*ins, *outs, *sc)
        if mid is not None:
            pl.when(step == nsteps - 2)(mid)
        pl.when(step == nsteps - 1)(end)

    all_aliases = dict(aliases or {})
    all_aliases.update({n_in + i: n_out + o for i, o in carry.aliases.items()})
    call = pl.pallas_call(
        wrapped, name=name, grid=grid, in_specs=list(in_specs) + [ANY] * k_in,
        out_specs=out_specs_l + [ANY] * k_out, out_shape=out_shape_l + list(carry.outs),
        scratch_shapes=list(scratch) + list(carry.scratch), input_output_aliases=all_aliases,
        compiler_params=params)

    def run(*args):
        res = call(*_in_hbm(args), *carry.ins)
        core = res[:n_out]
        return (core[0] if single else list(core)), list(res[n_out:])

    return run


def _full(shape):
    nd = len(shape)
    return pl.BlockSpec(shape, lambda *_: (0,) * nd)


ANY = pl.BlockSpec(memory_space=pl.ANY)


def _mm(a, b, *, name, mode="nn", out_dtype=F32, tm, tn, tk):
    if mode == "nn":
        (m, k), (k2, n) = a.shape, b.shape
    elif mode == "nt":
        (m, k), (n, k2) = a.shape, b.shape
    else:
        (k, m), (k2, n) = a.shape, b.shape
    assert k == k2 and m % tm == 0 and n % tn == 0 and k % tk == 0, (name, a.shape, b.shape)
    nk = k // tk
    dims = {"nn": NN, "nt": NT, "tn": TN}[mode]

    def body(a_ref, b_ref, o_ref, acc):
        kk = pl.program_id(2)

        @pl.when(kk == 0)
        def _():
            acc[...] = jnp.zeros_like(acc)

        acc[...] += _bdot(a_ref[...], b_ref[...], dims)

        @pl.when(kk == nk - 1)
        def _():
            o_ref[...] = acc[...].astype(out_dtype)

    a_spec = (pl.BlockSpec((tk, tm), lambda i, j, kk: (kk, i)) if mode == "tn"
              else pl.BlockSpec((tm, tk), lambda i, j, kk: (i, kk)))
    b_spec = (pl.BlockSpec((tn, tk), lambda i, j, kk: (j, kk)) if mode == "nt"
              else pl.BlockSpec((tk, tn), lambda i, j, kk: (kk, j)))
    return _pcall(body, name=name, grid=(m // tm, n // tn, nk), in_specs=[a_spec, b_spec],
                  out_specs=pl.BlockSpec((tm, tn), lambda i, j, kk: (i, j)),
                  out_shape=jax.ShapeDtypeStruct((m, n), out_dtype),
                  scratch=[pltpu.VMEM((tm, tn), F32)])(a, b)


NT_IN = NCOL // 256


def _src_block(j):
    return j + jnp.where(j < 4, 2, jnp.where(j < 6, 3, jnp.where(j < 8, -6, jnp.where(
        j < 16, 5, jnp.where(j < 20, -7, -14)))))


def _mm_in(h, wt, tm, carry=None):
    tt = h.shape[0]

    def body(h_ref, w_ref, o_ref):
        o_ref[...] = _bdot(h_ref[...], w_ref[...], NT)

    return _pcall(body, name="mm_in", grid=(tt // tm, NT_IN),
                  in_specs=[pl.BlockSpec((tm, D), lambda i, j: (i, 0)),
                            pl.BlockSpec((256, D), lambda i, j: (_src_block(j), 0))],
                  out_specs=pl.BlockSpec((tm, 256), lambda i, j: (i, j)),
                  out_shape=jax.ShapeDtypeStruct((tt, NCOL), F32), carry=carry)(h, wt)


def _mm_dh(dp, wt, tm, carry=None):
    tt = dp.shape[0]
    per, ng = 3, NT_IN // 3

    def body(d_ref, w0, w1, w2, o_ref, acc):
        kk = pl.program_id(1)

        @pl.when(kk == 0)
        def _():
            acc[...] = jnp.zeros_like(acc)

        acc[...] += (_bdot(d_ref[:, 0:256], w0[...]) + _bdot(d_ref[:, 256:512], w1[...])
                     + _bdot(d_ref[:, 512:768], w2[...]))

        @pl.when(kk == ng - 1)
        def _():
            o_ref[...] = acc[...]

    wspecs = [pl.BlockSpec((256, D), functools.partial(lambda t, i, kk: (_src_block(per * kk + t), 0), t))
              for t in range(per)]
    return _pcall(body, name="mm_dh", grid=(tt // tm, ng),
                  in_specs=[pl.BlockSpec((tm, per * 256), lambda i, kk: (i, kk))] + wspecs,
                  out_specs=pl.BlockSpec((tm, D), lambda i, kk: (i, 0)),
                  out_shape=jax.ShapeDtypeStruct((tt, D), F32), scratch=[pltpu.VMEM((tm, D), F32)],
                  carry=carry)(dp, wt, wt, wt)


def _mm_gin(dp, h, tk):
    tt = dp.shape[0]
    nk = tt // tk

    def body(d_ref, h_ref, o_ref, acc):
        kk = pl.program_id(1)

        @pl.when(kk == 0)
        def _():
            acc[...] = jnp.zeros_like(acc)

        acc[...] += _bdot(d_ref[...], h_ref[...], TN)

        @pl.when(kk == nk - 1)
        def _():
            o_ref[...] = acc[...]

    return _pcall(body, name="mm_gin", grid=(NT_IN, nk),
                  in_specs=[pl.BlockSpec((tk, 256), lambda j, kk: (kk, j)),
                            pl.BlockSpec((tk, D), lambda j, kk: (kk, 0))],
                  out_specs=pl.BlockSpec((256, D), lambda j, kk: (_src_block(j), 0)),
                  out_shape=jax.ShapeDtypeStruct((NCOL, D), F32), scratch=[pltpu.VMEM((256, D), F32)])(dp, h)


def _modulate(xin, nw, ss, *, name, sel):
    rows = xin.shape[0]

    def body(x_ref, nw_ref, ss_ref, h_ref):
        x = x_ref[...]
        r = lax.rsqrt(jnp.mean(x * x, axis=-1, keepdims=True) + EPS)
        s = ss_ref[0]
        h_ref[...] = ((x * r * nw_ref[...]) * (1.0 + s[1:2]) + s[0:1]).astype(BF16)

    return _pcall(body, name=name, grid=(rows // TM,),
                  in_specs=[pl.BlockSpec((TM, D), lambda i: (i, 0)), _full((1, D)),
                            pl.BlockSpec((1, 2, D), lambda i: (sel(i), 0, 0))],
                  out_specs=pl.BlockSpec((TM, D), lambda i: (i, 0)),
                  out_shape=jax.ShapeDtypeStruct((rows, D), BF16))(xin, nw, ss)


def _norm_bwd_rows(x, dh, nw, scale):
    r = lax.rsqrt(jnp.mean(x * x, axis=-1, keepdims=True) + EPS)
    xh = x * r
    dxh = dh * ((1.0 + scale) * nw)
    dx = r * (dxh - xh * jnp.mean(dxh * xh, axis=-1, keepdims=True))
    return dx, xh


def _res1_mod2(x, ao, g1, nw2, ss2):
    s_len = x.shape[0]

    def body(x_ref, ao_ref, g_ref, nw_ref, ss_ref, x1_ref, h_ref):
        x1 = x_ref[...] + g_ref[...] * ao_ref[...]
        x1_ref[...] = x1
        r = lax.rsqrt(jnp.mean(x1 * x1, axis=-1, keepdims=True) + EPS)
        s = ss_ref[0]
        h_ref[...] = ((x1 * r * nw_ref[...]) * (1.0 + s[1:2]) + s[0:1]).astype(BF16)

    row = pl.BlockSpec((TM, D), lambda i: (i, 0))
    return _pcall(body, name="res1_mod2", grid=(s_len // TM,),
                  in_specs=[row, row, _full((1, D)), _full((1, D)), _full((1, 2, D))],
                  out_specs=[row, row],
                  out_shape=[jax.ShapeDtypeStruct((s_len, D), F32),
                             jax.ShapeDtypeStruct((s_len, D), BF16)])(x, ao, g1, nw2, ss2)


TS = 1024


def _acc_call(body, *, name, grid, in_specs, out_specs, out_shape, acc_shapes, args):
    return _pcall(body, name=name, grid=grid, in_specs=in_specs, out_specs=out_specs, out_shape=out_shape,
                  scratch=[pltpu.VMEM(s, F32) for s in acc_shapes])(*args)


def _mm_cs(a, w4, *, name):
    m, k = a.shape
    _, _, ns = w4.shape

    def body(a_ref, w_ref, o_ref):
        o_ref[...] = _bdot(a_ref[...], w_ref[0])

    return _pcall(body, name=name, grid=(m // TS, 4),
                  in_specs=[pl.BlockSpec((TS, k), lambda i, j: (i, 0)),
                            pl.BlockSpec((1, k, ns), lambda i, j: (j, 0, 0))],
                  out_specs=pl.BlockSpec((TS, ns), lambda i, j: (i, j)),
                  out_shape=jax.ShapeDtypeStruct((m, 4 * ns), F32))(a, w4)


def _mm_cs_nt(a, w4, *, name):
    m = a.shape[0]
    _, k, ns = w4.shape

    def body(a_ref, w_ref, o_ref, acc):
        j = pl.program_id(1)

        @pl.when(j == 0)
        def _():
            acc[...] = jnp.zeros_like(acc)

        acc[...] += _bdot(a_ref[...], w_ref[0], NT)

        @pl.when(j == 3)
        def _():
            o_ref[...] = acc[...]

    return _acc_call(body, name=name, grid=(m // TS, 4),
                     in_specs=[pl.BlockSpec((TS, ns), lambda i, j: (i, j)),
                               pl.BlockSpec((1, k, ns), lambda i, j: (j, 0, 0))],
                     out_specs=pl.BlockSpec((TS, k), lambda i, j: (i, 0)),
                     out_shape=jax.ShapeDtypeStruct((m, k), F32), acc_shapes=[(TS, k)], args=(a, w4))


def _mm_cs_tn(a, b, ns, *, name):
    s_len, k = a.shape
    nk = s_len // TS

    def body(a_ref, b_ref, o_ref, acc):
        t = pl.program_id(1)

        @pl.when(t == 0)
        def _():
            acc[...] = jnp.zeros_like(acc)

        acc[...] += _bdot(a_ref[...], b_ref[...], TN)

        @pl.when(t == nk - 1)
        def _():
            o_ref[0] = acc[...]

    return _acc_call(body, name=name, grid=(4, nk),
                     in_specs=[pl.BlockSpec((TS, k), lambda j, t: (t, 0)),
                               pl.BlockSpec((TS, ns), lambda j, t: (t, j))],
                     out_specs=pl.BlockSpec((1, k, ns), lambda j, t: (j, 0, 0)),
                     out_shape=jax.ShapeDtypeStruct((4, k, ns), F32), acc_shapes=[(k, ns)], args=(a, b))


def _ffn_up(h2, g4, u4, carry=None):
    s_len = h2.shape[0]
    ns = g4.shape[1]

    def body(h_ref, g_ref, u_ref, a_ref, b_ref, z_ref):
        h = h_ref[...]
        a = _bdot(h, g_ref[0], NT)
        b = _bdot(h, u_ref[0], NT)
        a_ref[0] = a.astype(BF16)
        b_ref[0] = b.astype(BF16)
        z_ref[0] = (a * _sig(a) * b).astype(BF16)

    w = pl.BlockSpec((1, ns, D), lambda i, j: (j, 0, 0))
    o = pl.BlockSpec((1, TS, ns), lambda i, j: (j, i, 0))
    f = jax.ShapeDtypeStruct((4, s_len, ns), BF16)
    return _pcall(body, name="ffn_up", grid=(s_len // TS, 4),
                  in_specs=[pl.BlockSpec((TS, D), lambda i, j: (i, 0)), w, w], out_specs=[o, o, o],
                  out_shape=[f, f, jax.ShapeDtypeStruct((4, s_len, ns), BF16)], carry=carry)(h2, g4, u4)


def _ffn_down(z4, dn4):
    _, s_len, ns = z4.shape

    def body(z_ref, w_ref, o_ref, acc):
        j = pl.program_id(1)

        @pl.when(j == 0)
        def _():
            acc[...] = jnp.zeros_like(acc)

        acc[...] += _bdot(z_ref[0], w_ref[0])

        @pl.when(j == 3)
        def _():
            o_ref[...] = acc[...]

    return _acc_call(body, name="ffn_down", grid=(s_len // TS, 4),
                     in_specs=[pl.BlockSpec((1, TS, ns), lambda i, j: (j, i, 0)),
                               pl.BlockSpec((1, ns, D), lambda i, j: (j, 0, 0))],
                     out_specs=pl.BlockSpec((TS, D), lambda i, j: (i, 0)),
                     out_shape=jax.ShapeDtypeStruct((s_len, D), F32), acc_shapes=[(TS, D)], args=(z4, dn4))


def _ffn_dz(dyb, dn4, a4, b4):
    _, s_len, ns = a4.shape

    def body(dy_ref, w_ref, a_ref, b_ref, da_ref, db_ref):
        dz = _bdot(dy_ref[...], w_ref[0], NT)
        a = a_ref[0].astype(F32)
        s = _sig(a)
        da_ref[0] = (dz * b_ref[0].astype(F32) * (s * (1.0 + a * (1.0 - s)))).astype(BF16)
        db_ref[0] = (dz * (a * s)).astype(BF16)

    t = pl.BlockSpec((1, TS, ns), lambda i, j: (j, i, 0))
    o = jax.ShapeDtypeStruct((4, s_len, ns), BF16)
    return _pcall(body, name="ffn_dz", grid=(s_len // TS, 4),
                  in_specs=[pl.BlockSpec((TS, D), lambda i, j: (i, 0)),
                            pl.BlockSpec((1, ns, D), lambda i, j: (j, 0, 0)), t, t],
                  out_specs=[t, t], out_shape=[o, o])(dyb, dn4, a4, b4)


def _ffn_gdn(z4, dyb):
    _, s_len, ns = z4.shape
    nk = s_len // TS

    def body(z_ref, dy_ref, o_ref, acc):
        t = pl.program_id(1)

        @pl.when(t == 0)
        def _():
            acc[...] = jnp.zeros_like(acc)

        acc[...] += _bdot(z_ref[0], dy_ref[...], TN)

        @pl.when(t == nk - 1)
        def _():
            o_ref[0] = acc[...]

    return _acc_call(body, name="ffn_gdn", grid=(4, nk),
                     in_specs=[pl.BlockSpec((1, TS, ns), lambda j, t: (j, t, 0)),
                               pl.BlockSpec((TS, D), lambda j, t: (t, 0))],
                     out_specs=pl.BlockSpec((1, ns, D), lambda j, t: (j, 0, 0)),
                     out_shape=jax.ShapeDtypeStruct((4, ns, D), F32), acc_shapes=[(ns, D)], args=(z4, dyb))


def _ffn_dh2(da4, db4, g4, u4):
    _, s_len, ns = da4.shape

    def body(da_ref, db_ref, g_ref, u_ref, o_ref, acc):
        j = pl.program_id(1)

        @pl.when(j == 0)
        def _():
            acc[...] = jnp.zeros_like(acc)

        acc[...] += _bdot(da_ref[0], g_ref[0]) + _bdot(db_ref[0], u_ref[0])

        @pl.when(j == 3)
        def _():
            o_ref[...] = acc[...]

    t = pl.BlockSpec((1, TS, ns), lambda i, j: (j, i, 0))
    w = pl.BlockSpec((1, ns, D), lambda i, j: (j, 0, 0))
    return _acc_call(body, name="ffn_dh2", grid=(s_len // TS, 4), in_specs=[t, t, w, w],
                     out_specs=pl.BlockSpec((TS, D), lambda i, j: (i, 0)),
                     out_shape=jax.ShapeDtypeStruct((s_len, D), F32), acc_shapes=[(TS, D)],
                     args=(da4, db4, g4, u4))


def _ffn_ggu(h2, da4, db4):
    _, s_len, ns = da4.shape
    nk = s_len // TS

    def body(h_ref, da_ref, db_ref, gg_ref, gu_ref, acc_g, acc_u):
        t = pl.program_id(1)

        @pl.when(t == 0)
        def _():
            acc_g[...] = jnp.zeros_like(acc_g)
            acc_u[...] = jnp.zeros_like(acc_u)

        h = h_ref[...]
        acc_g[...] += _bdot(da_ref[0], h, TN)
        acc_u[...] += _bdot(db_ref[0], h, TN)

        @pl.when(t == nk - 1)
        def _():
            gg_ref[0] = acc_g[...]
            gu_ref[0] = acc_u[...]

    d = pl.BlockSpec((1, TS, ns), lambda j, t: (j, t, 0))
    o = pl.BlockSpec((1, ns, D), lambda j, t: (j, 0, 0))
    f = jax.ShapeDtypeStruct((4, ns, D), F32)
    return _acc_call(body, name="ffn_ggu", grid=(4, nk),
                     in_specs=[pl.BlockSpec((TS, D), lambda j, t: (t, 0)), d, d], out_specs=[o, o],
                     out_shape=[f, f], acc_shapes=[(ns, D), (ns, D)], args=(h2, da4, db4))


def _loss_head(x1, y, g2, tgt):
    s_len = x1.shape[0]

    def body(x1_ref, y_ref, g_ref, t_ref, sq_ref, dx2_ref, dyb_ref, dg_ref):
        i = pl.program_id(0)

        @pl.when(i == 0)
        def _():
            sq_ref[...] = jnp.zeros_like(sq_ref)
            dg_ref[...] = jnp.zeros_like(dg_ref)

        y_ = y_ref[...]
        g = g_ref[...]
        e = x1_ref[...] + g * y_ - t_ref[...]
        sq_ref[...] += jnp.sum(e * e, axis=0, keepdims=True)
        dx2 = e * (1.0 / D)
        dx2_ref[...] = dx2
        dyb_ref[...] = (g * dx2).astype(BF16)
        dg_ref[...] += jnp.sum(dx2 * y_, axis=0, keepdims=True)

    row = pl.BlockSpec((TM, D), lambda i: (i, 0))
    vec = _full((1, D))
    return _pcall(body, name="loss_head", grid=(s_len // TM,),
                  in_specs=[row, row, vec, row], out_specs=[vec, row, row, vec],
                  out_shape=[jax.ShapeDtypeStruct((1, D), F32), jax.ShapeDtypeStruct((s_len, D), F32),
                             jax.ShapeDtypeStruct((s_len, D), BF16),
                             jax.ShapeDtypeStruct((1, D), F32)])(x1, y, g2, tgt)


def _mod2_bwd(x1, dh2, dx2, ao, nw2, ss2, g1):
    s_len = x1.shape[0]

    def body(x1_ref, dh_ref, dx2_ref, ao_ref, nw_ref, ss_ref, g_ref,
             dx1_ref, da_ref, dss_ref, dnw_ref, dg_ref):
        i = pl.program_id(0)

        @pl.when(i == 0)
        def _():
            dss_ref[...] = jnp.zeros_like(dss_ref)
            dnw_ref[...] = jnp.zeros_like(dnw_ref)
            dg_ref[...] = jnp.zeros_like(dg_ref)

        dh = dh_ref[...]
        nw = nw_ref[...]
        scale = ss_ref[0][1:2]
        dxn, xh = _norm_bwd_rows(x1_ref[...], dh, nw, scale)
        dx1 = dx2_ref[...] + dxn
        dx1_ref[...] = dx1
        da_ref[...] = (g_ref[...] * dx1).astype(BF16)
        dg_ref[...] += jnp.sum(dx1 * ao_ref[...], axis=0, keepdims=True)
        dsh = jnp.sum(dh, axis=0, keepdims=True)
        dsc = jnp.sum(dh * xh * nw, axis=0, keepdims=True)
        dss_ref[...] += jnp.concatenate([dsh, dsc], axis=0)
        dnw_ref[...] += jnp.sum(dh * xh * (1.0 + scale), axis=0, keepdims=True)

    row = pl.BlockSpec((TM, D), lambda i: (i, 0))
    vec = _full((1, D))
    return _pcall(body, name="mod2_bwd", grid=(s_len // TM,),
                  in_specs=[row, row, row, row, vec, _full((1, 2, D)), vec],
                  out_specs=[row, row, _full((2, D)), vec, vec],
                  out_shape=[jax.ShapeDtypeStruct((s_len, D), F32), jax.ShapeDtypeStruct((s_len, D), BF16),
                             jax.ShapeDtypeStruct((2, D), F32), jax.ShapeDtypeStruct((1, D), F32),
                             jax.ShapeDtypeStruct((1, D), F32)])(x1, dh2, dx2, ao, nw2, ss2, g1)


def _mod1_bwd(tok, dh, dx1, nw1, ss1, carry=None):
    tt = tok.shape[0]
    s_len = dx1.shape[0]

    def body(t_ref, dh_ref, dx1_ref, nw_ref, ss_ref, dx_ref, dss_ref, dnw_ref):
        i = pl.program_id(0)

        @pl.when(i == 0)
        def _():
            dnw_ref[...] = jnp.zeros_like(dnw_ref)

        @pl.when(i <= 1)
        def _():
            dss_ref[...] = jnp.zeros_like(dss_ref)

        dh_ = dh_ref[...]
        nw = nw_ref[...]
        scale = ss_ref[0][1:2]
        dxn, xh = _norm_bwd_rows(t_ref[...], dh_, nw, scale)

        @pl.when(i >= 1)
        def _():
            dx_ref[...] = dx1_ref[...] + dxn

        dsh = jnp.sum(dh_, axis=0, keepdims=True)
        dsc = jnp.sum(dh_ * xh * nw, axis=0, keepdims=True)
        dss_ref[...] += jnp.concatenate([dsh, dsc], axis=0)[None]
        dnw_ref[...] += jnp.sum(dh_ * xh * (1.0 + scale), axis=0, keepdims=True)

    row = pl.BlockSpec((TM, D), lambda i: (i, 0))
    lat = pl.BlockSpec((TM, D), lambda i: (jnp.maximum(i - 1, 0), 0))
    sel = pl.BlockSpec((1, 2, D), lambda i: (jnp.minimum(i, 1), 0, 0))
    return _pcall(body, name="mod1_bwd", grid=(tt // TM,),
                  in_specs=[row, row, lat, _full((1, D)), sel],
                  out_specs=[lat, sel, _full((1, D))],
                  out_shape=[jax.ShapeDtypeStruct((s_len, D), F32), jax.ShapeDtypeStruct((2, 2, D), F32),
                             jax.ShapeDtypeStruct((1, D), F32)], carry=carry)(tok, dh, dx1, nw1, ss1)


def _rows(c):
    return slice(c * CH, (c + 1) * CH)


def _chunk_masks(rev, transpose=False):
    r = lax.broadcasted_iota(jnp.int32, (TM, TM), 0)
    c = lax.broadcasted_iota(jnp.int32, (TM, TM), 1)
    same = (r // CH) == (c // CH)
    before = (c >= r) if (rev != transpose) else (c <= r)
    return same & before, same


def _chunk_scan(x, rev, transpose=False):
    r = lax.broadcasted_iota(jnp.int32, (CH, CH), 0)
    c = lax.broadcasted_iota(jnp.int32, (CH, CH), 1)
    tri = ((c >= r) if (rev != transpose) else (c <= r)).astype(F32)
    return jnp.concatenate([_dot(tri, x[_rows(ch)], prec=HI) for ch in range(x.shape[0] // CH)], axis=0)


def _chunk_total(x):
    return jnp.concatenate([jnp.broadcast_to(jnp.sum(x[_rows(ch)], axis=0, keepdims=True), (CH, x.shape[1]))
                            for ch in range(x.shape[0] // CH)], axis=0)


def _hgrn_gate(fl, qraw, lg):
    lb = 1.0 / (1.0 + jnp.exp(lg[1:2] - lg[0:1]))
    sg = _sig(fl)
    f = lb + (1.0 - lb) * sg
    q = qraw * _sig(qraw) * (HGD ** -0.5)
    return lb, sg, f, q


def _hgrn_fwd(p, lg, *, rev, carry=None):
    tt = p.shape[0]
    nt = tt // TM
    ncht = TM // CH
    d = 1 if rev else 0

    def tile_of(s):
        return jnp.where(s == 0, 0, nt - s) if rev else s

    def body(f_ref, inp_ref, q_ref, lg_ref, o_ref, st_ref, state):
        s = pl.program_id(0)

        @pl.when(s == 0)
        def _():
            state[...] = jnp.zeros_like(state)

        _, _, f, q = _hgrn_gate(f_ref[...], q_ref[...], lg_ref[0])
        lf = jnp.log(f)
        causal, _ = _chunk_masks(rev)
        cum = _chunk_scan(lf, rev)
        tot = _chunk_total(lf)
        qd = (q * jnp.exp(cum)).astype(BF16)
        kd = ((1.0 - f) * jnp.exp(-cum)).astype(BF16)
        ke = ((1.0 - f) * jnp.exp(tot - cum)).astype(BF16)
        et = jnp.exp(tot)
        v = inp_ref[...].astype(BF16)
        order = range(ncht - 1, -1, -1) if rev else range(ncht)
        outs = []
        for h in range(4):
            sl = slice(h * HGD, (h + 1) * HGD)
            qd_, kd_, ke_, v_ = qd[:, sl], kd[:, sl], ke[:, sl], v[:, sl]
            pm = jnp.where(causal, _dot(qd_, kd_, NT), 0.0).astype(BF16)
            o_h = _dot(pm, v_)
            upd = [_dot(v_[_rows(c)], ke_[_rows(c)], TN) for c in range(ncht)]
            st = state[h]
            for c in order:
                st_ref[c, h] = st
                st = st * et[c * CH:c * CH + 1, sl] + upd[c]
            state[h] = st
            inter = [_dot(qd_[_rows(c)], st_ref[c, h].astype(BF16), NT) for c in range(ncht)]
            outs.append(o_h + jnp.concatenate(inter, axis=0))
        o_ref[...] = jnp.concatenate(outs, axis=1)

    def col(cb):
        return pl.BlockSpec((TM, HGW), lambda s: (tile_of(s), cb))

    return _pcall(
        body, name="hgrn_fwd_rev" if rev else "hgrn_fwd", grid=(nt,),
        in_specs=[col(C_FB if rev else C_FF), col(C_INP), col(C_QHG),
                  pl.BlockSpec((1, 2, HGW), lambda s: (d, 0, 0))],
        out_specs=[pl.BlockSpec((TM, HGW), lambda s: (tile_of(s), 0)),
                   pl.BlockSpec((ncht, 4, HGD, HGD), lambda s: (tile_of(s), 0, 0, 0))],
        out_shape=[jax.ShapeDtypeStruct((tt, HGW), F32),
                   jax.ShapeDtypeStruct((nt * ncht, 4, HGD, HGD), F32)],
        scratch=[pltpu.VMEM((4, HGD, HGD), F32)], carry=carry)(p, p, p, lg)


def _hgrn_bwd(p, lg, do, st, dp, prev, *, rev, carry=None):
    tt = p.shape[0]
    nt = tt // TM
    ncht = TM // CH
    d = 1 if rev else 0
    second = prev is not None

    def tile_of(s):
        return jnp.where(s == nt - 1, 0, s + 1) if rev else nt - 1 - s

    def body(*refs):
        if second:
            (f_ref, inp_ref, q_ref, lg_ref, do_ref, st_ref, dvp_ref, dqp_ref, _dp_in,
             dp_ref, dlg_ref, dstate) = refs
        else:
            (f_ref, inp_ref, q_ref, lg_ref, do_ref, st_ref, _dp_in,
             dp_ref, dv_ref, dq_ref, dlg_ref, dstate) = refs
        s = pl.program_id(0)
        tile = tile_of(s)

        @pl.when(s == 0)
        def _():
            dstate[...] = jnp.zeros_like(dstate)
            dlg_ref[...] = jnp.zeros_like(dlg_ref)

        qraw = q_ref[...]
        lb, sg, f, q = _hgrn_gate(f_ref[...], qraw, lg_ref[0])
        lf = jnp.log(f)
        causal, _ = _chunk_masks(rev)
        causal_t, _ = _chunk_masks(rev, transpose=True)
        cum = _chunk_scan(lf, rev)
        tot = _chunk_total(lf)
        ea, eb, ee, et = jnp.exp(cum), jnp.exp(-cum), jnp.exp(tot - cum), jnp.exp(tot)
        qdf, kdf, kef = q * ea, (1.0 - f) * eb, (1.0 - f) * ee
        qd, kd, ke = qdf.astype(BF16), kdf.astype(BF16), kef.astype(BF16)
        v = inp_ref[...].astype(BF16)
        dob = jnp.where(tile == 0, 0.0, do_ref[...]).astype(BF16)
        order = range(ncht) if rev else range(ncht - 1, -1, -1)
        dq_l, dk_l, dv_l, dcum_l, dtot_l = [], [], [], [], []
        for h in range(4):
            sl = slice(h * HGD, (h + 1) * HGD)
            qd_, kd_, ke_, v_, do_ = qd[:, sl], kd[:, sl], ke[:, sl], v[:, sl], dob[:, sl]
            pmt = jnp.where(causal_t, _dot(kd_, qd_, NT), 0.0).astype(BF16)
            dpm = jnp.where(causal, _dot(do_, v_, NT), 0.0).astype(BF16)
            dpmt = jnp.where(causal_t, _dot(v_, do_, NT), 0.0).astype(BF16)
            dv = _dot(pmt, do_)
            dqd = _dot(dpm, kd_)
            dkd = _dot(dpmt, qd_)
            upd = [_dot(do_[_rows(c)], qd_[_rows(c)], TN) for c in range(ncht)]
            ds = dstate[h]
            ds1 = [None] * ncht
            for c in order:
                ds1[c] = ds
                ds = ds * et[c * CH:c * CH + 1, sl] + upd[c]
            dstate[h] = ds
            dke_c, dv_c, dqd_c, dtot_c = [], [], [], []
            for c in range(ncht):
                st0 = st_ref[c, h]
                dsb = ds1[c].astype(BF16)
                dke_ = _dot(v_[_rows(c)], dsb)
                dke_c.append(dke_)
                dv_c.append(_dot(ke_[_rows(c)], dsb, NT))
                dqd_c.append(_dot(do_[_rows(c)], st0.astype(BF16)))
                dt = (jnp.sum(ds1[c] * st0, axis=0, keepdims=True) * et[c * CH:c * CH + 1, sl]
                      + jnp.sum(dke_ * kef[_rows(c), sl], axis=0, keepdims=True))
                dtot_c.append(jnp.broadcast_to(dt, (CH, HGD)))
            dke = jnp.concatenate(dke_c, axis=0)
            dqd = dqd + jnp.concatenate(dqd_c, axis=0)
            dv_l.append(dv + jnp.concatenate(dv_c, axis=0))
            dtot_l.append(jnp.concatenate(dtot_c, axis=0))
            dq_l.append(dqd * ea[:, sl])
            dk_l.append(dkd * eb[:, sl] + dke * ee[:, sl])
            dcum_l.append(dqd * qdf[:, sl] - dkd * kdf[:, sl] - dke * kef[:, sl])
        dcum = jnp.concatenate(dcum_l, axis=1)
        dlf = _chunk_scan(dcum, rev, transpose=True) + jnp.concatenate(dtot_l, axis=1)
        dq_t = jnp.concatenate(dq_l, axis=1)
        dv_t = jnp.concatenate(dv_l, axis=1)

        df = dlf / f - jnp.concatenate(dk_l, axis=1)
        dfl = df * (1.0 - lb) * sg * (1.0 - sg)
        dlb = jnp.sum(df * (1.0 - sg), axis=0, keepdims=True)
        dl0 = dlb * lb * (1.0 - lb)
        dlg_ref[...] += jnp.concatenate([dl0, -dl0], axis=0)[None]
        if second:
            sq = _sig(qraw)
            dqr = (dqp_ref[...] + dq_t) * (HGD ** -0.5) * (sq * (1.0 + qraw * (1.0 - sq)))
            dp_ref[...] = jnp.concatenate([dfl, dvp_ref[...] + dv_t, dqr], axis=1).astype(BF16)
        else:
            dp_ref[...] = dfl.astype(BF16)
            dv_ref[...] = dv_t
            dq_ref[...] = dq_t

    def col(cb):
        return pl.BlockSpec((TM, HGW), lambda s: (tile_of(s), cb))

    tok = pl.BlockSpec((TM, HGW), lambda s: (tile_of(s), 0))
    in_specs = [col(C_FB if rev else C_FF), col(C_INP), col(C_QHG),
                pl.BlockSpec((1, 2, HGW), lambda s: (d, 0, 0)),
                pl.BlockSpec((TM, HGW), lambda s: (jnp.maximum(tile_of(s) - 1, 0), 0)),
                pl.BlockSpec((ncht, 4, HGD, HGD), lambda s: (tile_of(s), 0, 0, 0))]
    args = [p, p, p, lg, do, st]
    dlg_spec = _full((1, 2, HGW))
    dlg_shape = jax.ShapeDtypeStruct((1, 2, HGW), F32)
    if second:
        in_specs += [tok, tok]
        args += [prev[0], prev[1]]
        out_specs = [pl.BlockSpec((TM, 3 * HGW), lambda s: (tile_of(s), 0)), dlg_spec]
        out_shape = [jax.ShapeDtypeStruct(dp.shape, BF16), dlg_shape]
    else:
        out_specs = [pl.BlockSpec((TM, HGW), lambda s: (tile_of(s), C_FB if rev else C_FF)), tok, tok, dlg_spec]
        out_shape = [jax.ShapeDtypeStruct(dp.shape, BF16), jax.ShapeDtypeStruct((tt, HGW), F32),
                     jax.ShapeDtypeStruct((tt, HGW), F32), dlg_shape]
    in_specs.append(ANY)
    args.append(dp)
    return _pcall(body, name="hgrn_bwd_rev" if rev else "hgrn_bwd", grid=(nt,),
                  in_specs=in_specs, out_specs=out_specs, out_shape=out_shape,
                  scratch=[pltpu.VMEM((4, HGD, HGD), F32)],
                  aliases={len(args) - 1: 0}, carry=carry)(*args)


def _head_rms(o, w, nheads):
    outs = []
    for h in range(nheads):
        oh = o[:, h * HGD:(h + 1) * HGD]
        outs.append(oh * lax.rsqrt(jnp.mean(oh * oh, axis=-1, keepdims=True) + EPS))
    return jnp.concatenate(outs, axis=1)


def _readout(o0, o1, p, hw4):
    s_len = o0.shape[0] - L

    def body(o0_ref, o1_ref, g_ref, w_ref, y_ref):
        xh = _head_rms(o0_ref[...] + o1_ref[...], None, 4)
        g = g_ref[...]
        y_ref[...] = (xh * w_ref[...] * (g * _sig(g))).astype(BF16)

    lat = pl.BlockSpec((TM, HGW), lambda i: (i + 1, 0))
    return _pcall(body, name="readout", grid=(s_len // TM,),
                  in_specs=[lat, lat, pl.BlockSpec((TM, HGW), lambda i: (i + 1, C_GHG)), _full((1, HGW))],
                  out_specs=pl.BlockSpec((TM, HGW), lambda i: (i, 0)),
                  out_shape=jax.ShapeDtypeStruct((s_len, HGW), BF16))(o0, o1, p, hw4)


def _readout_bwd(o0, o1, p, hw4, dy, dp, carry=None):
    tt = o0.shape[0]
    s_len = tt - L

    def body(o0_ref, o1_ref, g_ref, w_ref, dy_ref, _dp_in, dp_ref, do_ref, dw_ref):
        i = pl.program_id(0)

        @pl.when(i == 0)
        def _():
            dw_ref[...] = jnp.zeros_like(dw_ref)
            dp_ref[...] = jnp.zeros_like(dp_ref)

        @pl.when(i >= 1)
        def _():
            o = o0_ref[...] + o1_ref[...]
            g = g_ref[...]
            w = w_ref[...]
            sg = _sig(g)
            dy_ = dy_ref[...]
            dsw = dy_ * (g * sg)
            outs, xhs = [], []
            for h in range(4):
                sl = slice(h * HGD, (h + 1) * HGD)
                oh = o[:, sl]
                r = lax.rsqrt(jnp.mean(oh * oh, axis=-1, keepdims=True) + EPS)
                xh = oh * r
                dxh = dsw[:, sl] * w[:, sl]
                outs.append(r * (dxh - xh * jnp.mean(dxh * xh, axis=-1, keepdims=True)))
                xhs.append(xh)
            xh = jnp.concatenate(xhs, axis=1)
            do_ref[...] = jnp.concatenate(outs, axis=1)
            dp_ref[...] = (dy_ * xh * w * (sg * (1.0 + g * (1.0 - sg)))).astype(BF16)
            dw_ref[...] += jnp.sum(dsw * xh, axis=0, keepdims=True)

    tok = pl.BlockSpec((TM, HGW), lambda i: (i, 0))
    lat = pl.BlockSpec((TM, HGW), lambda i: (jnp.maximum(i - 1, 0), 0))
    return _pcall(body, name="readout_bwd", grid=(tt // TM,),
                  in_specs=[tok, tok, pl.BlockSpec((TM, HGW), lambda i: (i, C_GHG)), _full((1, HGW)), lat, ANY],
                  out_specs=[pl.BlockSpec((TM, HGW), lambda i: (i, C_GHG)), lat, _full((1, HGW))],
                  out_shape=[jax.ShapeDtypeStruct(dp.shape, BF16), jax.ShapeDtypeStruct((s_len, HGW), F32),
                             jax.ShapeDtypeStruct((1, HGW), F32)],
                  aliases={5: 0}, carry=carry)(o0, o1, p, hw4, dy, dp)


def _rope_tables(s_len):
    t = np.arange(s_len)
    inv = ROPE_THETA ** (-np.arange(0, 32, 2, dtype=np.float64) / 32)
    def half(pos):
        ang = pos[:, None].astype(np.float64) * inv[None, :]
        return (np.concatenate([np.cos(ang), np.cos(ang)], 1), np.concatenate([-np.sin(ang), np.sin(ang)], 1))
    cr, sr = half(t // GRID_W)
    cc, sc = half(t % GRID_W)
    cos = np.concatenate([cr, cc, cr, cc], 1)
    sin = np.concatenate([sr, sc, sr, sc], 1)
    cos = np.concatenate([np.ones((L, 128)), cos], 0)
    sin = np.concatenate([np.zeros((L, 128)), sin], 0)
    return jnp.asarray(cos, F32), jnp.asarray(sin, F32)


def _blockdiag(n, w):
    i = np.arange(n)
    return jnp.asarray((i[:, None] // w == i[None, :] // w) / float(w), F32)


def _dup_matrix():
    m = np.zeros((128, 512), np.float32)
    for g in range(2):
        for j in range(4):
            for dd in range(HDIM):
                m[64 * g + dd, 256 * g + 64 * j + dd] = 1.0
    return m


def _head_mean(x, blockdiag):
    return _dot(x, blockdiag, prec=lax.Precision.HIGH)


def _rot(x):
    n = x.shape[1]
    lane = lax.broadcasted_iota(jnp.int32, x.shape, 1)
    return jnp.where((lane % 32) < 16, pltpu.roll(x, n - 16, 1), pltpu.roll(x, 16, 1))


def _qk_prep(p, cos, sin, qnw8, knw2, bd512, bd128, dup):
    tt = p.shape[0]

    def body(q_ref, kv_ref, cos_ref, sin_ref, qw_ref, kw_ref, b5_ref, b1_ref, dup_ref,
             qr_ref, k4_ref, v4_ref):
        cos_, sin_ = cos_ref[...], sin_ref[...]
        q = q_ref[...]
        qn = q * lax.rsqrt(_head_mean(q * q, b5_ref[...]) + EPS) * qw_ref[...]
        cos4 = jnp.concatenate([cos_] * 4, axis=1)
        sin4 = jnp.concatenate([sin_] * 4, axis=1)
        qr_ref[...] = ((qn * cos4 + _rot(qn) * sin4) * (HDIM ** -0.5)).astype(BF16)
        kv = kv_ref[...]
        k, v = kv[:, :128], kv[:, 128:]
        kn = k * lax.rsqrt(_head_mean(k * k, b1_ref[...]) + EPS) * kw_ref[...]
        kr = kn * cos_ + _rot(kn) * sin_
        k4_ref[...] = _bdot(kr, dup_ref[...]).astype(BF16)
        v4_ref[...] = _bdot(v, dup_ref[...]).astype(BF16)

    row = lambda w, cb: pl.BlockSpec((TM, w), lambda i: (i, cb))
    out = jax.ShapeDtypeStruct((tt, ATW), BF16)
    return _pcall(body, name="qk_prep", grid=(tt // TM,),
                  in_specs=[row(ATW, C_QRAW), row(256, C_KV), row(128, 0), row(128, 0),
                            _full((1, ATW)), _full((1, 128)), _full((ATW, ATW)), _full((128, 128)),
                            _full((128, ATW))],
                  out_specs=[row(ATW, 0)] * 3, out_shape=[out] * 3)(
                      p, p, cos, sin, qnw8, knw2, bd512, bd128, dup)


def _attn_masks(i, nb):
    r = lax.broadcasted_iota(jnp.int32, (4 * BLK, 3 * BLK + L), 0) % BLK
    c = lax.broadcasted_iota(jnp.int32, (4 * BLK, 3 * BLK + L), 1)
    kpos = (i - 1) * BLK + c
    loc = (jnp.abs(c - BLK - r) <= BLK) & (kpos >= 0) & (kpos < nb * BLK)
    return loc | (c >= 3 * BLK)


def _stack_mask():
    r = lax.broadcasted_iota(jnp.int32, (4 * BLK, 256), 0)
    lane = lax.broadcasted_iota(jnp.int32, (4 * BLK, 256), 1)
    return (r // BLK) == (lane // HDIM)


def _stack_heads(xg, fill=0.0):
    x4 = jnp.concatenate([xg] * 4, axis=0)
    return jnp.where(_stack_mask(), x4, jnp.full_like(x4, fill))


def _unstack_heads(x4):
    out = jnp.where(_lane_mask(0), x4[0:BLK], 0.0)
    for j in range(1, 4):
        out = out + jnp.where(_lane_mask(j), x4[j * BLK:(j + 1) * BLK], 0.0)
    return out


def _per_head_rows(vals):
    return jnp.concatenate([jnp.broadcast_to(v, (BLK, 1)) for v in vals], axis=0)


def _lane_mask(j):
    lane = lax.broadcasted_iota(jnp.int32, (1, 256), 1)
    return (lane // HDIM) == j


def _attn_specs(nb):
    blk = lambda off: pl.BlockSpec((BLK, ATW), lambda i: (jnp.clip(i + off, 0, nb - 1) + 2, 0))
    ctx = pl.BlockSpec((L, ATW), lambda i: (0, 0))
    return blk, ctx


def _attn_fwd(qr, k4, v4, sinks, carry=None):
    tt = qr.shape[0]
    s_len = tt - L
    nb = s_len // BLK

    def body(sk_ref, q_ref, kp, ko, kn, kc, vp, vo, vn, vc, y_ref, lse_ref):
        i = pl.program_id(0)
        valid = _attn_masks(i, nb)
        q = q_ref[...]
        ys, lses = [], []
        for g in range(2):
            gs = slice(256 * g, 256 * g + 256)
            kcat = jnp.concatenate([kp[:, gs], ko[:, gs], kn[:, gs], kc[:, gs]], axis=0)
            vcat = jnp.concatenate([vp[:, gs], vo[:, gs], vn[:, gs], vc[:, gs]], axis=0)
            sink = _per_head_rows([sk_ref[4 * g + j] for j in range(4)])
            s = jnp.where(valid, _dot(_stack_heads(q[:, gs]), kcat, NT), -1e30)
            m = jnp.maximum(jnp.max(s, axis=-1, keepdims=True), sink)
            e = jnp.exp(s - m)
            den = jnp.sum(e, axis=-1, keepdims=True) + jnp.exp(sink - m)
            ys.append(_unstack_heads(_bdot(e / den, vcat)))
            lses.append(_unstack_heads(jnp.broadcast_to(m + jnp.log(den), (4 * BLK, 256))))
        y_ref[...] = jnp.concatenate(ys, axis=1).astype(BF16)
        lse_ref[...] = jnp.concatenate(lses, axis=1)

    blk, ctx = _attn_specs(nb)
    out = pl.BlockSpec((BLK, ATW), lambda i: (i, 0))
    return _pcall(body, name="attn_fwd", grid=(nb,),
                  in_specs=[pl.BlockSpec(memory_space=pltpu.SMEM), blk(0),
                            blk(-1), blk(0), blk(1), ctx, blk(-1), blk(0), blk(1), ctx],
                  out_specs=[out, out],
                  out_shape=[jax.ShapeDtypeStruct((s_len, ATW), BF16),
                             jax.ShapeDtypeStruct((s_len, ATW), F32)], carry=carry)(
                      sinks, qr, k4, k4, k4, k4, v4, v4, v4, v4)


def _attn_bwd(qr, k4, v4, sinks, y, lse, dy, carry=None):
    tt = qr.shape[0]
    s_len = tt - L
    nb = s_len // BLK

    def body(sk_ref, q_ref, kp, ko, kn, kc, vp, vo, vn, vc, y_ref, lse_ref, dy_ref,
             dq_ref, dkw_ref, dvw_ref, dkc_ref, dvc_ref, dsk_ref):
        i = pl.program_id(0)

        @pl.when(i == 0)
        def _():
            dkc_ref[...] = jnp.zeros_like(dkc_ref)
            dvc_ref[...] = jnp.zeros_like(dvc_ref)
            dsk_ref[...] = jnp.zeros_like(dsk_ref)

        valid = _attn_masks(i, nb)
        q = q_ref[...]
        dy_ = dy_ref[...]
        dly = dy_ * y_ref[...].astype(F32)
        lse_ = lse_ref[...]
        dqs = []
        for g in range(2):
            gs = slice(256 * g, 256 * g + 256)
            kcat = jnp.concatenate([kp[:, gs], ko[:, gs], kn[:, gs], kc[:, gs]], axis=0)
            vcat = jnp.concatenate([vp[:, gs], vo[:, gs], vn[:, gs], vc[:, gs]], axis=0)
            q4 = _stack_heads(q[:, gs])
            dy4 = _stack_heads(dy_[:, gs]).astype(BF16)
            lse4 = jnp.max(_stack_heads(lse_[:, gs], fill=-1e30), axis=-1, keepdims=True)
            delta = jnp.sum(_stack_heads(dly[:, gs]), axis=-1, keepdims=True)
            sink = _per_head_rows([sk_ref[4 * g + j] for j in range(4)])
            pr = jnp.where(valid, jnp.exp(_dot(q4, kcat, NT) - lse4), 0.0)
            dsb = (pr * (_dot(dy4, vcat, NT) - delta)).astype(BF16)
            dsink = jnp.exp(sink - lse4) * delta
            for j in range(4):
                dsk_ref[4 * g + j:4 * g + j + 1, :] += jnp.broadcast_to(
                    -jnp.sum(dsink[j * BLK:(j + 1) * BLK], axis=0, keepdims=True), (1, 128))
            dqs.append(_unstack_heads(_dot(dsb, kcat)))
            dkg = _dot(dsb, q4, TN)
            dvg = _dot(pr.astype(BF16), dy4, TN)
            dkw_ref[0, :, gs] = dkg[:3 * BLK]
            dvw_ref[0, :, gs] = dvg[:3 * BLK]
            dkc_ref[:, gs] += dkg[3 * BLK:]
            dvc_ref[:, gs] += dvg[3 * BLK:]
        dq_ref[...] = jnp.concatenate(dqs, axis=1)

    blk, ctx = _attn_specs(nb)
    out = pl.BlockSpec((BLK, ATW), lambda i: (i, 0))
    win = pl.BlockSpec((1, 3 * BLK, ATW), lambda i: (i, 0, 0))
    acc = _full((L, ATW))
    return _pcall(body, name="attn_bwd", grid=(nb,),
                  in_specs=[pl.BlockSpec(memory_space=pltpu.SMEM), blk(0),
                            blk(-1), blk(0), blk(1), ctx, blk(-1), blk(0), blk(1), ctx, out, out, out],
                  out_specs=[out, win, win, acc, acc, _full((8, 128))],
                  out_shape=[jax.ShapeDtypeStruct((s_len, ATW), F32),
                             jax.ShapeDtypeStruct((nb, 3 * BLK, ATW), F32),
                             jax.ShapeDtypeStruct((nb, 3 * BLK, ATW), F32),
                             jax.ShapeDtypeStruct((L, ATW), F32), jax.ShapeDtypeStruct((L, ATW), F32),
                             jax.ShapeDtypeStruct((8, 128), F32)], carry=carry)(
                      sinks, qr, k4, k4, k4, k4, v4, v4, v4, v4, y, lse, dy)


def _attn_post(p, cos, sin, qnw8, knw2, bd512, bd128, dupt, dq, dkw, dvw, dkc, dvc, dp, carry=None):
    tt = p.shape[0]
    s_len = tt - L
    nb = s_len // BLK
    nctx = L // BLK

    def body(q_ref, kv_ref, cos_ref, sin_ref, qw_ref, kw_ref, b5_ref, b1_ref, dupt_ref,
             dq_ref, kwp, kwo, kwn, vwp, vwo, vwn, dkc_ref, dvc_ref, _dp_in,
             dp_ref, dqw_ref, dkw_ref):
        t = pl.program_id(0)
        j = t - nctx

        @pl.when(t == 0)
        def _():
            dqw_ref[...] = jnp.zeros_like(dqw_ref)
            dkw_ref[...] = jnp.zeros_like(dkw_ref)

        is_lat = t >= nctx
        cos_, sin_ = cos_ref[...], sin_ref[...]
        has_p = is_lat & (j >= 1)
        has_n = is_lat & (j <= nb - 2)
        dk4 = (jnp.where(is_lat, kwo[0], dkc_ref[...]) + jnp.where(has_p, kwp[0], 0.0)
               + jnp.where(has_n, kwn[0], 0.0))
        dv4 = (jnp.where(is_lat, vwo[0], dvc_ref[...]) + jnp.where(has_p, vwp[0], 0.0)
               + jnp.where(has_n, vwn[0], 0.0))
        dkr = _dot(dk4, dupt_ref[...], prec=HI)
        dv = _dot(dv4, dupt_ref[...], prec=HI)
        kv = kv_ref[...]
        k = kv[:, :128]
        kw = kw_ref[...]
        rk = lax.rsqrt(_head_mean(k * k, b1_ref[...]) + EPS)
        xk = k * rk
        dkn = dkr * cos_ + _rot(dkr * sin_)
        dxk = dkn * kw
        dk = rk * (dxk - xk * _head_mean(dxk * xk, b1_ref[...]))
        dkw_ref[...] += jnp.sum(dkn * xk, axis=0, keepdims=True)
        q = q_ref[...]
        qw = qw_ref[...]
        rq = lax.rsqrt(_head_mean(q * q, b5_ref[...]) + EPS)
        xq = q * rq
        cos4 = jnp.concatenate([cos_] * 4, axis=1)
        sin4 = jnp.concatenate([sin_] * 4, axis=1)
        dqr = jnp.where(is_lat, dq_ref[...], 0.0) * (HDIM ** -0.5)
        dqn = dqr * cos4 + _rot(dqr * sin4)
        dxq = dqn * qw
        dqraw = rq * (dxq - xq * _head_mean(dxq * xq, b5_ref[...]))
        dqw_ref[...] += jnp.sum(dqn * xq, axis=0, keepdims=True)
        dp_ref[...] = jnp.concatenate([dqraw, dk, dv], axis=1).astype(BF16)

    row = lambda w, cb: pl.BlockSpec((BLK, w), lambda t: (t, cb))
    lat = pl.BlockSpec((BLK, ATW), lambda t: (jnp.maximum(t - nctx, 0), 0))

    def part(off):
        return pl.BlockSpec((1, BLK, ATW), lambda t: (jnp.clip(t - nctx + off, 0, nb - 1), 1 - off, 0))

    cacc = pl.BlockSpec((BLK, ATW), lambda t: (jnp.minimum(t, nctx - 1), 0))
    return _pcall(body, name="attn_post", grid=(tt // BLK,),
                  in_specs=[row(ATW, C_QRAW), row(256, C_KV), row(128, 0), row(128, 0),
                            _full((1, ATW)), _full((1, 128)), _full((ATW, ATW)), _full((128, 128)),
                            _full((ATW, 128)), lat, part(-1), part(0), part(1), part(-1), part(0), part(1),
                            cacc, cacc, ANY],
                  out_specs=[pl.BlockSpec((BLK, 768), lambda t: (t, C_QKV)), _full((1, ATW)), _full((1, 128))],
                  out_shape=[jax.ShapeDtypeStruct(dp.shape, BF16), jax.ShapeDtypeStruct((1, ATW), F32),
                             jax.ShapeDtypeStruct((1, 128), F32)],
                  aliases={18: 0}, carry=carry)(p, p, cos, sin, qnw8, knw2, bd512, bd128, dupt,
                                   dq, dkw, dkw, dkw, dvw, dvw, dvw, dkc, dvc, dp)


def _merge(ah, aa, p):
    s_len = ah.shape[0]

    def body(ah_ref, aa_ref, gh_ref, ga_ref, m_ref):
        m_ref[...] = (_sig(gh_ref[...]) * ah_ref[...] + _sig(ga_ref[...]) * aa_ref[...]).astype(BF16)

    row = pl.BlockSpec((TM, D), lambda i: (i, 0))
    return _pcall(body, name="merge", grid=(s_len // TM,),
                  in_specs=[row, row, pl.BlockSpec((TM, D), lambda i: (i + 1, 2)),
                            pl.BlockSpec((TM, D), lambda i: (i + 1, 3))],
                  out_specs=row, out_shape=jax.ShapeDtypeStruct((s_len, D), BF16))(ah, aa, p, p)


def _merge_bwd(dm, ah, aa, p, carry=None):
    tt = p.shape[0]
    s_len = tt - L

    def body(dm_ref, ah_ref, aa_ref, gh_ref, ga_ref, dp_ref, dmh_ref, dma_ref):
        i = pl.program_id(0)

        @pl.when(i == 0)
        def _():
            dp_ref[...] = jnp.zeros_like(dp_ref)

        @pl.when(i >= 1)
        def _():
            dm_ = dm_ref[...]
            sh, sa = _sig(gh_ref[...]), _sig(ga_ref[...])
            dp_ref[...] = jnp.concatenate([dm_ * ah_ref[...] * sh * (1.0 - sh),
                                           dm_ * aa_ref[...] * sa * (1.0 - sa)], axis=1).astype(BF16)
            dmh_ref[...] = (dm_ * sh).astype(BF16)
            dma_ref[...] = (dm_ * sa).astype(BF16)

    lat = pl.BlockSpec((TM, D), lambda i: (jnp.maximum(i - 1, 0), 0))
    return _pcall(body, name="merge_bwd", grid=(tt // TM,),
                  in_specs=[lat, lat, lat, pl.BlockSpec((TM, D), lambda i: (i, 2)),
                            pl.BlockSpec((TM, D), lambda i: (i, 3))],
                  out_specs=[pl.BlockSpec((TM, 2 * D), lambda i: (i, C_GATES)), lat, lat],
                  out_shape=[jax.ShapeDtypeStruct((tt, NCOL), BF16), jax.ShapeDtypeStruct((s_len, D), BF16),
                             jax.ShapeDtypeStruct((s_len, D), BF16)], carry=carry)(dm, ah, aa, p, p)


def _local_step(x, ctx, tgt, mod, modc, nw1, nw2, lg, hw, qnw, knw, sinks,
                w_in, wts, dist=None):
    s_len = x.shape[0]
    tt = s_len + L
    tok = jnp.concatenate([ctx, x], axis=0)
    ss1 = jnp.stack([modc, mod[0:2]])
    ss2 = mod[3:5][None]
    g1, g2 = mod[2:3], mod[5:6]
    hw4 = jnp.tile(hw, (1, 4))
    qnw8 = jnp.tile(qnw, (1, 8))
    knw2 = jnp.tile(knw, (1, 2))
    cos, sin = _rope_tables(s_len)
    bd512, bd128 = _blockdiag(ATW, HDIM), _blockdiag(128, HDIM)
    dupm = _dup_matrix()
    dup, dupt = jnp.asarray(dupm, BF16), jnp.asarray(dupm.T, F32)
    tmt = tt

    def four(b):
        return b.reshape(4, 2 * b.shape[1], b.shape[2])

    def halves(g):
        return g.reshape(4, 2, g.shape[1] // 2, g.shape[2])

    h = _modulate(tok, nw1, ss1, name="mod1", sel=lambda i: jnp.minimum(i, 1))
    if dist is None:
        bh4, ba4, w_o, g4, u4, dn4 = wts
        p = _mm_in(h, w_in, tmt)
        o0, st0 = _hgrn_fwd(p, lg, rev=False)
        o1, st1 = _hgrn_fwd(p, lg, rev=True)
    else:
        core, chip = dist
        half = wts[3].shape[1] // 2
        p, first = _mm_in(h, w_in, tmt, carry=_carry_join(_carry_gather(list(wts[0:3])),
                                                          _carry_gather([wts[3]], rows=[(0, half)])))
        (o0, st0), (g8,) = _hgrn_fwd(p, lg, rev=False, carry=_carry_gather([first[3]], rows=[(half, half)]))
        o1, st1 = _hgrn_fwd(p, lg, rev=True)
        bh4, ba4, w_o, g4 = four(first[0]), four(first[1]), four(first[2]).reshape(D, D), four(g8)
    y_hg = _readout(o0, o1, p, hw4)
    qr, k4, v4 = _qk_prep(p, cos, sin, qnw8, knw2, bd512, bd128, dup)
    if dist is None:
        y_at, lse = _attn_fwd(qr, k4, v4, sinks)
    else:
        (y_at, lse), (u8,) = _attn_fwd(qr, k4, v4, sinks, carry=_carry_gather([wts[4]]))
        u4 = four(u8)
    ah = _mm_cs(y_hg, bh4, name="mm_bh")
    aa = _mm_cs(y_at, ba4, name="mm_ba")
    mixed = _merge(ah, aa, p)
    ao = _mm(mixed, w_o, name="mm_o", tm=512, tn=D, tk=D)
    x1, h2 = _res1_mod2(x, ao, g1, nw2, ss2)
    if dist is None:
        a4, b4, z4 = _ffn_up(h2, g4, u4)
    else:
        (a4, b4, z4), (dn8,) = _ffn_up(h2, g4, u4, carry=_carry_gather([wts[5]]))
        dn4 = four(dn8)
    y = _ffn_down(z4, dn4)
    sq, dx2, dyb, dg2 = _loss_head(x1, y, g2, tgt)

    da4, db4 = _ffn_dz(dyb, dn4, a4, b4)
    g_dn = _ffn_gdn(z4, dyb)
    dh2 = _ffn_dh2(da4, db4, g4, u4)
    g_g, g_u = _ffn_ggu(h2, da4, db4)
    dx1, dattn, dss2, dnw2, dg1 = _mod2_bwd(x1, dh2, dx2, ao, nw2, ss2, g1)
    dm = _mm(dattn, w_o, name="mm_dm", mode="nt", tm=512, tn=D, tk=D)
    g_o = _mm(mixed, dattn, name="mm_go", mode="tn", tm=D, tn=D, tk=512)
    if dist is None:
        dp, dmh, dma = _merge_bwd(dm, ah, aa, p)
    else:
        ffn_units = [halves(g_dn), halves(g_g), halves(g_u)]
        (dp, dmh, dma), ffn_recv = _merge_bwd(dm, ah, aa, p, carry=_carry_pairx(ffn_units))
        ffn_pairs = _rs_pair_add(ffn_units, ffn_recv, core)
    dy_hg = _mm_cs_nt(dmh, bh4, name="mm_dyh")
    dy_at = _mm_cs_nt(dma, ba4, name="mm_dya")
    g_bh = _mm_cs_tn(y_hg, dmh, D // 4, name="mm_gbh")
    g_ba = _mm_cs_tn(y_at, dma, D // 4, name="mm_gba")
    if dist is None:
        dp, do, dhw4 = _readout_bwd(o0, o1, p, hw4, dy_hg, dp)
        dq, dkw, dvw, dkc, dvc, dsk = _attn_bwd(qr, k4, v4, sinks, y_at, lse, dy_at)
        dp, dqnw8, dknw2 = _attn_post(p, cos, sin, qnw8, knw2, bd512, bd128, dupt, dq, dkw, dvw, dkc, dvc, dp)
    else:
        mix_units = [halves(g_bh), halves(g_ba), halves(g_o.reshape(4, D // 4, D))]
        (dp, do, dhw4), mix_recv = _readout_bwd(o0, o1, p, hw4, dy_hg, dp, carry=_carry_pairx(mix_units))
        mix_pairs = _rs_pair_add(mix_units, mix_recv, core)
        (dq, dkw, dvw, dkc, dvc, dsk), c_dn = _attn_bwd(qr, k4, v4, sinks, y_at, lse, dy_at,
                                                        carry=_carry_chipx(ffn_pairs[0:1]))
        red_dn = _rs_chip_add(ffn_pairs[0:1], c_dn, core, chip)
        (dp, dqnw8, dknw2), post = _attn_post(
            p, cos, sin, qnw8, knw2, bd512, bd128, dupt, dq, dkw, dvw, dkc, dvc, dp,
            carry=_carry_join(_carry_chipx(ffn_pairs[1:2]), _carry_sibx(red_dn)))
        red_g = _rs_chip_add(ffn_pairs[1:2], post[0:1], core, chip)
    if dist is None:
        dp, dv0, dq0, dlg0 = _hgrn_bwd(p, lg, do, st0, dp, None, rev=False)
        dp, dlg1 = _hgrn_bwd(p, lg, do, st1, dp, (dv0, dq0), rev=True)
    else:
        (dp, dv0, dq0, dlg0), mid = _hgrn_bwd(p, lg, do, st0, dp, None, rev=False,
                                              carry=_carry_join(_carry_chipx(ffn_pairs[2:3]), _carry_sibx(red_g)))
        red_u = _rs_chip_add(ffn_pairs[2:3], mid[0:1], core, chip)
        (dp, dlg1), last = _hgrn_bwd(p, lg, do, st1, dp, (dv0, dq0), rev=True,
                                     carry=_carry_join(_carry_chipx(mix_pairs), _carry_sibx(red_u)))
        mix_reds = _rs_chip_add(mix_pairs, last[0:3], core, chip)
        ffn_done = post[1:2] + mid[1:2] + last[3:4]
    if dist is None:
        dh = _mm_dh(dp, w_in, tmt)
    else:
        dh, mix_done = _mm_dh(dp, w_in, tmt, carry=_carry_sibx(mix_reds))
    g_in = _mm_gin(dp, h, tmt)
    if dist is None:
        gx, dss1, dnw1 = _mod1_bwd(tok, dh, dx1, nw1, ss1)
        rs = None
    else:
        in_units = [halves(g_in.reshape(4, NCOL // 4, D))]
        (gx, dss1, dnw1), in_recv = _mod1_bwd(tok, dh, dx1, nw1, ss1, carry=_carry_pairx(in_units))
        rs = dict(ffn_done=ffn_done, mix_done=mix_done, in_units=in_units, in_recv=in_recv)

    dmod = jnp.concatenate([dss1[1], dg1, dss2, dg2], axis=0)
    dmodc = dss1[0]
    raw = (dss1, dg1, dss2, dg2, dnw1, dnw2, dhw4, dqnw8, dknw2, dsk, dlg0, dlg1)
    small = dict(raw=raw, dmod=dmod, dmodc=dmodc, dnw1=dnw1, dnw2=dnw2,
                 dhw=dhw4.reshape(4, HGD).sum(0, keepdims=True),
                 dqnw=dqnw8.reshape(8, HDIM).sum(0, keepdims=True),
                 dknw=dknw2.reshape(2, HDIM).sum(0, keepdims=True),
                 dsinks=dsk[:, 0], dlg=jnp.concatenate([dlg0, dlg1], axis=0))
    big = dict(w_in=g_in, w_bh=g_bh, w_ba=g_ba, w_o=g_o, w_g=g_g, w_u=g_u, w_dn=g_dn)
    return sq, gx, big, small, rs


def _place():
    x, y, c = lax.axis_index("x"), lax.axis_index("y"), lax.axis_index("c")
    return x, y, c


def _gather_blocks(x_refs, out_refs, send_sems, recv_sems, local_sems):
    n = len(out_refs)
    x, y, c = _place()
    me, sibling = (x, y, c), (x, y, 1 - c)
    chips = [(1 - x, y), (x, 1 - y), (1 - x, 1 - y)]

    def slot(u, px, py, pc):
        return out_refs[u].at[4 * px + 2 * py + pc]

    def copy(u, k, block, to, src=None):
        return pltpu.make_async_remote_copy(
            src_ref=slot(u, *block) if src is None else src, dst_ref=slot(u, *block),
            send_sem=send_sems.at[u, k], recv_sem=recv_sems.at[u, k], device_id=to, device_id_type=MESH)

    mines = []
    if x_refs is not None:
        mines = [pltpu.make_async_copy(x_refs[u], slot(u, *me), local_sems.at[u]) for u in range(n)]
    for cp in mines:
        cp.start()
    first = []
    for u in range(n):
        src = None if x_refs is None else x_refs[u]
        first.append(copy(u, 0, me, sibling, src=src))
        first += [copy(u, 1 + j, me, (*chip, c), src=src) for j, chip in enumerate(chips)]
    for cp in first:
        cp.start()
    passed = []
    for j, chip in enumerate(chips):
        for u in range(n):
            copy(u, 1 + j, (*chip, c), me).wait_recv()
            fwd = copy(u, 4 + j, (*chip, c), sibling)
            fwd.start()
            passed.append(fwd)
    for u in range(n):
        copy(u, 0, sibling, me).wait_recv()
    for j, chip in enumerate(chips):
        for u in range(n):
            copy(u, 4 + j, (*chip, 1 - c), me).wait_recv()
    for cp in first + passed:
        cp.wait_send()
    for cp in mines:
        cp.wait()


def _gather_sems(n):
    return [pltpu.SemaphoreType.DMA((n, 7)), pltpu.SemaphoreType.DMA((n, 7)), pltpu.SemaphoreType.DMA((n,))]


def _allgather(blks, *, name, in_vmem):
    n = len(blks)
    space = pltpu.VMEM if in_vmem else pl.ANY

    def body(*refs):
        _gather_blocks(refs[:n], refs[n:2 * n], *refs[2 * n:])

    return pl.pallas_call(
        body, name=name, out_shape=[jax.ShapeDtypeStruct((8,) + b.shape, b.dtype) for b in blks],
        in_specs=[pl.BlockSpec(memory_space=space)] * n, out_specs=[pl.BlockSpec(memory_space=space)] * n,
        scratch_shapes=_gather_sems(n))(*blks)


def _cast_place(ws, c, dev):
    n = len(ws)

    def body(s_ref, *refs):
        for u in range(n):
            refs[n + u][0] = refs[u][...].astype(BF16)

    in_specs, out_specs, out_shape = [], [], []
    for w in ws:
        q, cols = w.shape[0] // 4, w.shape[1]
        in_specs.append(pl.BlockSpec((q, cols), lambda i, s: (2 * s[0] + i, 0)))
        out_specs.append(pl.BlockSpec((1, q, cols), lambda i, s: (s[1], i, 0)))
        out_shape.append(jax.ShapeDtypeStruct((8, 2 * q, cols), BF16))
    return pl.pallas_call(
        body, name="cast_place",
        grid_spec=pltpu.PrefetchScalarGridSpec(num_scalar_prefetch=1, grid=(2,), in_specs=in_specs,
                                               out_specs=out_specs),
        out_shape=out_shape,
        compiler_params=pltpu.CompilerParams(vmem_limit_bytes=48 << 20))(jnp.stack([c, dev]), *ws)


def _gather_phases(out_refs, send_sems, recv_sems, rows=None):
    n = len(out_refs)
    x, y, c = _place()
    me, sibling = (x, y, c), (x, y, 1 - c)
    chips = [(1 - x, y), (x, 1 - y), (1 - x, 1 - y)]

    def copy(u, k, block, to):
        px, py, pc = block
        ref = out_refs[u].at[4 * px + 2 * py + pc]
        if rows is not None and rows[u] is not None:
            ref = ref.at[pl.ds(rows[u][0], rows[u][1])]
        return pltpu.make_async_remote_copy(src_ref=ref, dst_ref=ref, send_sem=send_sems.at[u, k],
                                            recv_sem=recv_sems.at[u, k], device_id=to, device_id_type=MESH)

    def start():
        for u in range(n):
            copy(u, 0, me, sibling).start()
            for j, chip in enumerate(chips):
                copy(u, 1 + j, me, (*chip, c)).start()

    def mid():
        for j, chip in enumerate(chips):
            for u in range(n):
                copy(u, 1 + j, (*chip, c), me).wait_recv()
                copy(u, 4 + j, (*chip, c), sibling).start()

    def end():
        for u in range(n):
            copy(u, 0, sibling, me).wait_recv()
        for j, chip in enumerate(chips):
            for u in range(n):
                copy(u, 4 + j, (*chip, 1 - c), me).wait_recv()
        for u in range(n):
            copy(u, 0, me, sibling).wait_send()
            for j, chip in enumerate(chips):
                copy(u, 1 + j, me, (*chip, c)).wait_send()
                copy(u, 4 + j, (*chip, c), sibling).wait_send()

    return start, mid, end


def _carry_gather(bufs, rows=None):
    n = len(bufs)
    return _Carry(bufs, [jax.ShapeDtypeStruct(b.shape, b.dtype) for b in bufs], {u: u for u in range(n)},
                  [pltpu.SemaphoreType.DMA((n, 7)), pltpu.SemaphoreType.DMA((n, 7))],
                  lambda ins, outs, sems: _gather_phases(outs, *sems, rows=rows))


def _allgather_inplace(bufs, *, name):
    n = len(bufs)

    def body(*refs):
        for phase in _gather_phases(refs[n:2 * n], *refs[2 * n:]):
            phase()

    return pl.pallas_call(
        body, name=name, out_shape=[jax.ShapeDtypeStruct(b.shape, b.dtype) for b in bufs],
        in_specs=[ANY] * n, out_specs=[ANY] * n, input_output_aliases={u: u for u in range(n)},
        scratch_shapes=[pltpu.SemaphoreType.DMA((n, 7)), pltpu.SemaphoreType.DMA((n, 7))])(*bufs)


def _ag_small(raw):
    def body(dss1, dg1, dss2, dg2, dnw1, dnw2, dhw4, dqnw8, dknw2, dsk, dlg0, dlg1,
             out_ref, tot_ref, blk, send_sems, recv_sems, local_sems):
        blk[...] = jnp.zeros_like(blk)
        blk[0:2, :] = dss1[1]
        blk[2:3, :] = dg1[...]
        blk[3:5, :] = dss2[...]
        blk[5:6, :] = dg2[...]
        blk[6:8, :] = dss1[0]
        blk[8:9, :] = dnw1[...]
        blk[9:10, :] = dnw2[...]
        blk[10:11, 0:HGW] = dhw4[...]
        blk[10:11, HGW:D] = dqnw8[...]
        blk[11:12, 0:128] = dknw2[...]
        blk[12:14, 0:HGW] = dlg0[0]
        blk[14:16, 0:HGW] = dlg1[0]
        blk[16:24, 0:128] = dsk[...]
        _gather_blocks([blk], [out_ref], send_sems, recv_sems, local_sems)
        acc = out_ref[0]
        for i in range(1, 8):
            acc = acc + out_ref[i]
        tot_ref[...] = acc

    vm = pl.BlockSpec(memory_space=pltpu.VMEM)
    return pl.pallas_call(
        body, name="ag_small",
        out_shape=[jax.ShapeDtypeStruct((8, 24, D), F32), jax.ShapeDtypeStruct((24, D), F32)],
        in_specs=[vm] * 12, out_specs=[vm, vm],
        scratch_shapes=[pltpu.VMEM((24, D), F32)] + _gather_sems(1))(*raw)


def _rs_pair_exchange(units):
    n = len(units)

    def body(*refs):
        start, _, end = _pairx_phases(refs[:n], refs[n:2 * n], *refs[2 * n:])
        start()
        end()

    return pl.pallas_call(
        body, name="rs_pair_exchange", out_shape=_pairx_shapes(units),
        in_specs=[ANY] * n, out_specs=[ANY] * n,
        scratch_shapes=[pltpu.SemaphoreType.DMA((n, 4)), pltpu.SemaphoreType.DMA((n, 4))])(*units)


def _pairx_shapes(units):
    return [jax.ShapeDtypeStruct((4,) + g.shape[2:], g.dtype) for g in units]


def _pairx_phases(g_refs, r_refs, send_sems, recv_sems):
    n = len(g_refs)
    x, y, c = _place()
    cps = [pltpu.make_async_remote_copy(
        src_ref=g_refs[u].at[j, 1 - c], dst_ref=r_refs[u].at[j], send_sem=send_sems.at[u, j],
        recv_sem=recv_sems.at[u, j], device_id=(x, y, 1 - c), device_id_type=MESH)
        for u in range(n) for j in range(4)]

    def start():
        for cp in cps:
            cp.start()

    def end():
        for cp in cps:
            cp.wait()

    return start, None, end


def _carry_pairx(units):
    n = len(units)
    return _Carry(units, _pairx_shapes(units), {},
                  [pltpu.SemaphoreType.DMA((n, 4)), pltpu.SemaphoreType.DMA((n, 4))],
                  lambda ins, outs, sems: _pairx_phases(ins, outs, *sems))


def _rs_pair_add(units, recvs, c):
    n = len(units)

    def body(c_ref, *refs):
        for u in range(n):
            refs[2 * n + u][...] = (refs[u][0] + refs[n + u][...]).astype(BF16)

    in_specs, out_specs, out_shape = [], [], []
    for g in units:
        h, w = g.shape[2] // 2, g.shape[3]
        in_specs.append(pl.BlockSpec((1, 1, h, w), lambda j, i, cr: (j, cr[0], i, 0)))
    for g in units:
        h, w = g.shape[2] // 2, g.shape[3]
        in_specs.append(pl.BlockSpec((1, h, w), lambda j, i, cr: (j, i, 0)))
        out_specs.append(pl.BlockSpec((1, h, w), lambda j, i, cr: (j, i, 0)))
        out_shape.append(jax.ShapeDtypeStruct((4, 2 * h, w), BF16))
    return pl.pallas_call(
        body, name="rs_pair_add",
        grid_spec=pltpu.PrefetchScalarGridSpec(num_scalar_prefetch=1, grid=(4, 2), in_specs=in_specs,
                                               out_specs=out_specs),
        out_shape=out_shape,
        compiler_params=pltpu.CompilerParams(vmem_limit_bytes=48 << 20))(c.reshape(1), *units, *recvs)


def _rs_chip_exchange(pairs):
    n = len(pairs)

    def body(*refs):
        start, _, end = _chipx_phases(refs[:n], refs[n:2 * n], *refs[2 * n:])
        start()
        end()

    return pl.pallas_call(
        body, name="rs_chip_exchange", out_shape=[jax.ShapeDtypeStruct(p.shape, p.dtype) for p in pairs],
        in_specs=[ANY] * n, out_specs=[ANY] * n,
        scratch_shapes=[pltpu.SemaphoreType.DMA((n, 3)), pltpu.SemaphoreType.DMA((n, 3))])(*pairs)


def _chipx_phases(p_refs, r_refs, send_sems, recv_sems):
    n = len(p_refs)
    x, y, c = _place()
    k = 2 * x + y
    sends = []
    for d in range(1, 4):
        j = (k + d) % 4
        for u in range(n):
            sends.append(pltpu.make_async_remote_copy(
                src_ref=p_refs[u].at[j], dst_ref=r_refs[u].at[k], send_sem=send_sems.at[u, d - 1],
                recv_sem=recv_sems.at[u, d - 1], device_id=(j // 2, j % 2, c), device_id_type=MESH))

    def start():
        for cp in sends:
            cp.start()

    def end():
        for d in range(1, 4):
            src = (k + 4 - d) % 4
            for u in range(n):
                pltpu.make_async_remote_copy(
                    src_ref=p_refs[u].at[src], dst_ref=r_refs[u].at[src], send_sem=send_sems.at[u, d - 1],
                    recv_sem=recv_sems.at[u, d - 1], device_id=(x, y, c), device_id_type=MESH).wait_recv()
        for cp in sends:
            cp.wait_send()

    return start, None, end


def _carry_chipx(pairs):
    n = len(pairs)
    return _Carry(pairs, [jax.ShapeDtypeStruct(p.shape, p.dtype) for p in pairs], {},
                  [pltpu.SemaphoreType.DMA((n, 3)), pltpu.SemaphoreType.DMA((n, 3))],
                  lambda ins, outs, sems: _chipx_phases(ins, outs, *sems))


def _rs_chip_add(pairs, contribs, c, chip):
    n = len(pairs)

    def body(s_ref, *refs):
        for u in range(n):
            a, b, c_, d = refs[4 * u:4 * u + 4]
            refs[4 * n + u][0] = ((a[0].astype(F32) + b[0].astype(F32)) + c_[0].astype(F32)) + d[0].astype(F32)

    in_specs, out_specs, out_shape, args = [], [], [], []
    for p, r in zip(pairs, contribs):
        h, w = p.shape[1] // 2, p.shape[2]
        in_specs += [pl.BlockSpec((1, h, w), functools.partial(lambda d, i, s: ((s[1] + d) % 4, i, 0), d))
                     for d in range(4)]
        args += [p, r, r, r]
        out_specs.append(pl.BlockSpec((1, h, w), lambda i, s: (s[0], i, 0)))
        out_shape.append(jax.ShapeDtypeStruct((2, 2 * h, w), F32))
    return pl.pallas_call(
        body, name="rs_chip_add",
        grid_spec=pltpu.PrefetchScalarGridSpec(num_scalar_prefetch=1, grid=(2,), in_specs=in_specs,
                                               out_specs=out_specs),
        out_shape=out_shape,
        compiler_params=pltpu.CompilerParams(vmem_limit_bytes=48 << 20))(jnp.stack([c, chip]), *args)


def _rs_sibling_gather(reds):
    n = len(reds)

    def body(*refs):
        start, _, end = _sibx_phases(refs[n:2 * n], *refs[2 * n:])
        start()
        end()

    return pl.pallas_call(
        body, name="rs_sibling_gather", out_shape=[jax.ShapeDtypeStruct(r.shape, r.dtype) for r in reds],
        in_specs=[ANY] * n, out_specs=[ANY] * n, input_output_aliases={u: u for u in range(n)},
        scratch_shapes=[pltpu.SemaphoreType.DMA((n,))] * 2)(*reds)


def _sibx_phases(o_refs, send_sems, recv_sems):
    n = len(o_refs)
    x, y, c = _place()
    cps = [pltpu.make_async_remote_copy(
        src_ref=o_refs[u].at[c], dst_ref=o_refs[u].at[c], send_sem=send_sems.at[u], recv_sem=recv_sems.at[u],
        device_id=(x, y, 1 - c), device_id_type=MESH) for u in range(n)]

    def start():
        for cp in cps:
            cp.start()

    def end():
        for u in range(n):
            cps[u].wait_send()
            pltpu.make_async_remote_copy(
                src_ref=o_refs[u].at[1 - c], dst_ref=o_refs[u].at[1 - c], send_sem=send_sems.at[u],
                recv_sem=recv_sems.at[u], device_id=(x, y, 1 - c), device_id_type=MESH).wait_recv()

    return start, None, end


def _carry_sibx(reds):
    n = len(reds)
    return _Carry(reds, [jax.ShapeDtypeStruct(r.shape, r.dtype) for r in reds], {u: u for u in range(n)},
                  [pltpu.SemaphoreType.DMA((n,))] * 2, lambda ins, outs, sems: _sibx_phases(outs, *sems))


def _prologue(blk, c_ctx, w, b, in8):
    n = w.shape[1]

    def body(blk_ref, cctx_ref, w_ref, b_ref, _in_in, g0_ref, c16_ref, g1_ref, in_ref, mod_s,
             s1, r1, l1, s2, r2, l2, s3, r3):
        start, mid, end = _gather_phases([in_ref], s3, r3)
        start()
        _gather_blocks([blk_ref], [g0_ref], s1, r1, l1)
        c16 = jnp.concatenate([g0_ref[i, 0:1, :] for i in range(8)] + [cctx_ref[...], jnp.zeros((7, D), F32)],
                              axis=0)
        c16_ref[...] = c16
        mod_s[...] = _dot(c16 * _sig(c16), w_ref[...], prec=HI) + b_ref[...]
        _gather_blocks([mod_s], [g1_ref], s2, r2, l2)
        mid()
        end()

    vm = pl.BlockSpec(memory_space=pltpu.VMEM)
    return pl.pallas_call(
        body, name="prologue",
        out_shape=[jax.ShapeDtypeStruct((8, 8, D), F32), jax.ShapeDtypeStruct((16, D), F32),
                   jax.ShapeDtypeStruct((8, 16, n), F32), jax.ShapeDtypeStruct(in8.shape, in8.dtype)],
        in_specs=[vm, vm, vm, vm, ANY], out_specs=[vm, vm, vm, ANY], input_output_aliases={4: 3},
        scratch_shapes=[pltpu.VMEM((16, n), F32)] + _gather_sems(1) + _gather_sems(1)
        + [pltpu.SemaphoreType.DMA((1, 7)), pltpu.SemaphoreType.DMA((1, 7))],
        compiler_params=pltpu.CompilerParams(vmem_limit_bytes=48 << 20))(blk, c_ctx, w, b, in8)


def _ada_bwd(c16, dmod16, w):
    n = w.shape[1]
    tn = 512

    def body(c_ref, d_ref, w_ref, gw_ref, gc_ref):
        j = pl.program_id(0)

        @pl.when(j == 0)
        def _():
            gc_ref[...] = jnp.zeros_like(gc_ref)

        cc = c_ref[...]
        dm = d_ref[...]
        gw_ref[...] = _dot(cc * _sig(cc), dm, TN, prec=HI)
        gc_ref[...] += _dot(dm, w_ref[...], NT, prec=HI)

    return _pcall(body, name="ada_bwd", grid=(n // tn,),
                  in_specs=[_full((16, D)), pl.BlockSpec((16, tn), lambda j: (0, j)),
                            pl.BlockSpec((D, tn), lambda j: (0, j))],
                  out_specs=[pl.BlockSpec((D, tn), lambda j: (0, j)), _full((16, D))],
                  out_shape=[jax.ShapeDtypeStruct((D, n), F32),
                             jax.ShapeDtypeStruct((16, D), F32)])(c16, dmod16, w)


def _adam_math(w, g, m, v):
    c1 = 1.0 - ADAM_B1 ** ADAM_STEP
    c2 = 1.0 - ADAM_B2 ** ADAM_STEP
    nm = ADAM_B1 * m + (1.0 - ADAM_B1) * g
    nv = ADAM_B2 * v + (1.0 - ADAM_B2) * (g * g)
    return -ADAM_LR * ((nm / c1) / (jnp.sqrt(nv / c2) + ADAM_EPS) + ADAM_WD * w), nm, nv


def _adamw_small(ws, gs, ms, vs):
    n = len(ws)

    def body(*refs):
        for u in range(n):
            d_, nm, nv = _adam_math(refs[u][...], refs[n + u][...], refs[2 * n + u][...], refs[3 * n + u][...])
            refs[4 * n + u][...] = d_
            refs[5 * n + u][...] = nm
            refs[6 * n + u][...] = nv

    specs = [_full(w.shape) for w in ws]
    shapes = [jax.ShapeDtypeStruct(w.shape, F32) for w in ws]
    out = _pcall(body, name="adamw_small", grid=(1,), in_specs=specs * 4, out_specs=specs * 3,
                 out_shape=shapes * 3)(*ws, *gs, *ms, *vs)
    return out[:n], out[n:2 * n], out[2 * n:]


def _cctx_grad(parts, c_ctx):
    def body(p_ref, c_ref, o_ref):
        acc = p_ref[0:1, :]
        for k in range(1, 4):
            acc = acc + p_ref[k:k + 1, :]
        cc = c_ref[...]
        s = _sig(cc)
        o_ref[...] = acc * (s * (1.0 + cc * (1.0 - s)))

    return _pcall(body, name="cctx_grad", grid=(1,), in_specs=[_full(parts.shape), _full((1, D))],
                  out_specs=_full((1, D)), out_shape=jax.ShapeDtypeStruct((1, D), F32))(parts, c_ctx)


ADAM_STEPS = 8


def _adamw_multi(ws, gs, ms, vs, *, name, carry=None):
    n = len(ws)

    def body(*refs):
        for u in range(n):
            refs[4 * n + u][...], refs[5 * n + u][...], refs[6 * n + u][...] = _adam_math(
                refs[u][...], refs[n + u][...], refs[2 * n + u][...], refs[3 * n + u][...])

    specs = [pl.BlockSpec((w.shape[0] // ADAM_STEPS, w.shape[1]), lambda i: (i, 0)) for w in ws]
    shapes = [jax.ShapeDtypeStruct(w.shape, F32) for w in ws]
    res = _pcall(body, name=name, grid=(ADAM_STEPS,), in_specs=specs * 4, out_specs=specs * 3,
                 out_shape=shapes * 3, carry=carry)(*ws, *gs, *ms, *vs)
    out, extra = res if carry is not None else (res, None)
    return (out[:n], out[n:2 * n], out[2 * n:]), extra


def kernel(x, c, ctx, c_ctx, w_ada, b_ada, norm_mix_w, norm_ffn_w, w_in, hgrn_lb_logits, hgrn_norm_w, q_norm_w, k_norm_w, attn_sinks, w_branch_hgrn, w_branch_attn, w_out, w_ffn_gate, w_ffn_up, w_ffn_down, loss_target, m_c_ctx, m_w_ada, m_b_ada, m_norm_mix_w, m_norm_ffn_w, m_w_in, m_hgrn_lb_logits, m_hgrn_norm_w, m_q_norm_w, m_k_norm_w, m_attn_sinks, m_w_branch_hgrn, m_w_branch_attn, m_w_out, m_w_ffn_gate, m_w_ffn_up, m_w_ffn_down, v_c_ctx, v_w_ada, v_b_ada, v_norm_mix_w, v_norm_ffn_w, v_w_in, v_hgrn_lb_logits, v_hgrn_norm_w, v_q_norm_w, v_k_norm_w, v_attn_sinks, v_w_branch_hgrn, v_w_branch_attn, v_w_out, v_w_ffn_gate, v_w_ffn_up, v_w_ffn_down):
    xi, yi, ci = _place()
    chip = 2 * xi + yi
    dev = 2 * chip + ci
    s_len = x.shape[1]

    shards = [w_in[0].T, w_branch_hgrn[0], w_branch_attn[0], w_out[0], w_ffn_gate[0].T, w_ffn_up[0].T,
              w_ffn_down[0]]
    bufs = _cast_place(shards, ci, dev)

    lbrow = jnp.pad(hgrn_lb_logits.reshape(1, 512), ((0, 0), (0, D - 512)))
    blk = jnp.concatenate([c, lbrow, jnp.zeros((6, D), F32)], axis=0)
    nada = w_ada.shape[2]
    b_sh = lax.dynamic_slice(b_ada, (0, chip * nada), (1, nada))
    g0, c16, g1, in8 = _prologue(blk, c_ctx[None], w_ada[0], b_sh, bufs[0])
    lg = g0[0::2, 1, :512].reshape(4, 2, 2, 128).transpose(1, 2, 0, 3).reshape(2, 2, HGW)
    modall = g1[0::2].transpose(1, 0, 2).reshape(16, 4 * nada)
    mod = lax.dynamic_slice(modall, (dev, 0), (1, 6 * D)).reshape(6, D)
    modc = modall[8].reshape(6, D)[:2]

    sq, gx, _, small, rs = _local_step(
        x[0], ctx[0], loss_target[0], mod, modc, norm_mix_w, norm_ffn_w, lg, hgrn_norm_w, q_norm_w,
        k_norm_w, attn_sinks[0], in8.reshape(NCOL, D), bufs[1:], dist=(ci, chip))
    loss = lax.psum(0.5 * jnp.sum(sq) / D, ("x", "y", "c"))

    def whole(r):
        return r.reshape(2 * r.shape[1], r.shape[2])

    g_dn, g_g, g_u = [whole(r) for r in rs["ffn_done"]]
    g_bh, g_ba, g_o = [whole(r) for r in rs["mix_done"]]
    in_pairs = _rs_pair_add(rs["in_units"], rs["in_recv"], ci)

    g2, tot = _ag_small(small["raw"])
    dmodc_tot = jnp.pad(tot[6:8].reshape(1, 2 * D), ((0, 0), (0, 4 * D)))
    g_b_ada = tot[0:6].reshape(1, 6 * D) + dmodc_tot
    dmod16 = jnp.concatenate([g2[:, 0:6].reshape(8, 6 * D), dmodc_tot, jnp.zeros((7, 6 * D), F32)], axis=0)
    g_w_ada, gc_part = _ada_bwd(c16, lax.dynamic_slice(dmod16, (0, chip * nada), (16, nada)), w_ada[0])
    g3, = _allgather([gc_part[8:16]], name="ag_cctx", in_vmem=True)
    g_c_ctx = _cctx_grad(g3[0::2, 0], c_ctx[None])[0]
    g_nw1 = tot[8:9]
    g_nw2 = tot[9:10]
    g_hw = tot[10, :HGW].reshape(4, HGD).sum(0, keepdims=True)
    g_qnw = tot[10, HGW:].reshape(8, HDIM).sum(0, keepdims=True)
    g_knw = tot[11, :128].reshape(2, HDIM).sum(0, keepdims=True)
    g_sinks = tot[16:24, 0][None]
    g_lg = lax.dynamic_slice(tot[12:16, :HGW].reshape(2, 2, HGW), (0, 0, chip * 128), (2, 2, 128))

    names = ["c_ctx", "w_ada", "b_ada", "norm_mix_w", "norm_ffn_w", "w_in", "hgrn_lb_logits", "hgrn_norm_w",
             "q_norm_w", "k_norm_w", "attn_sinks", "w_branch_hgrn", "w_branch_attn", "w_out", "w_ffn_gate",
             "w_ffn_up", "w_ffn_down"]
    ws = dict(zip(names, [c_ctx, w_ada, b_ada, norm_mix_w, norm_ffn_w, w_in, hgrn_lb_logits, hgrn_norm_w,
                          q_norm_w, k_norm_w, attn_sinks, w_branch_hgrn, w_branch_attn, w_out, w_ffn_gate,
                          w_ffn_up, w_ffn_down]))
    ms = dict(zip(names, [m_c_ctx, m_w_ada, m_b_ada, m_norm_mix_w, m_norm_ffn_w, m_w_in, m_hgrn_lb_logits,
                          m_hgrn_norm_w, m_q_norm_w, m_k_norm_w, m_attn_sinks, m_w_branch_hgrn,
                          m_w_branch_attn, m_w_out, m_w_ffn_gate, m_w_ffn_up, m_w_ffn_down]))
    vs = dict(zip(names, [v_c_ctx, v_w_ada, v_b_ada, v_norm_mix_w, v_norm_ffn_w, v_w_in, v_hgrn_lb_logits,
                          v_hgrn_norm_w, v_q_norm_w, v_k_norm_w, v_attn_sinks, v_w_branch_hgrn,
                          v_w_branch_attn, v_w_out, v_w_ffn_gate, v_w_ffn_up, v_w_ffn_down]))
    transposed = ("w_in", "w_ffn_gate", "w_ffn_up")

    def view(a, n):
        return a[0].T if n in transposed else a[0]

    def unview(a, n):
        return a.T[None] if n in transposed else a[None]

    delta, new_m, new_v, grads = {}, {}, {}, {}

    def big_adamw(group, gs, name, carry=None):
        (d_, m_, v_), extra = _adamw_multi([view(ws[n], n) for n in group], gs, [view(ms[n], n) for n in group],
                                           [view(vs[n], n) for n in group], name=name, carry=carry)
        for i, n in enumerate(group):
            grads[n], delta[n], new_m[n], new_v[n] = (unview(gs[i], n), unview(d_[i], n), unview(m_[i], n),
                                                      unview(v_[i], n))
        return extra

    in_contribs = big_adamw(["w_ffn_down", "w_ffn_gate", "w_ffn_up", "w_out", "w_branch_hgrn", "w_branch_attn"],
                            [g_dn, g_g, g_u, g_o, g_bh, g_ba], "adamw_first", carry=_carry_chipx(in_pairs))
    in_reds = _rs_chip_add(in_pairs, in_contribs, ci, chip)
    g_in, = [whole(r) for r in _rs_sibling_gather(in_reds)]
    big_adamw(["w_in", "w_ada"], [g_in, g_w_ada], "adamw_second")
    grads.update(c_ctx=g_c_ctx, b_ada=g_b_ada, norm_mix_w=g_nw1, norm_ffn_w=g_nw2, hgrn_lb_logits=g_lg,
                 hgrn_norm_w=g_hw, q_norm_w=g_qnw, k_norm_w=g_knw, attn_sinks=g_sinks)
    small_names = [n for n in names if n not in delta]

    def two_d(a):
        return a.reshape(1, -1) if a.ndim == 1 else a

    sd, sm_, sv = _adamw_small(*[[two_d(d[n]) for n in small_names] for d in (ws, grads, ms, vs)])
    for i, n in enumerate(small_names):
        for dst, src in ((delta, sd), (new_m, sm_), (new_v, sv)):
            dst[n] = src[i].reshape(ws[n].shape)
    return (loss, gx[None], *[grads[n] for n in names], *[delta[n] for n in names],
            *[new_m[n] for n in names], *[new_v[n] for n in names])
```

```python
import functools

import numpy as np
import jax
import jax.numpy as jnp
from jax import lax
from jax.experimental import pallas as pl
from jax.experimental.pallas import tpu as pltpu

F32 = jnp.float32
BF16 = jnp.bfloat16
HI = lax.Precision.HIGHEST
MESH = pl.DeviceIdType.MESH

D = 1024
L = 256
TM = 256
HGW = 512
HGD = 128
CH = 32
ATW = 512
HDIM = 64
BLK = 128
GRID_W = 64
DFF = 2816
NCOL = 5376
EPS = 1e-6
ROPE_THETA = 10000.0

C_FB, C_INP, C_QHG, C_FF = 0, 1, 2, 3
C_GATES = 1
C_GHG, C_QRAW = 8, 9
C_KV = 20
C_QKV = 6

ADAM_LR, ADAM_B1, ADAM_B2, ADAM_EPS, ADAM_WD, ADAM_STEP = 0.001, 0.9, 0.999, 1e-08, 0.01, 10

NN = (((1,), (0,)), ((), ()))
NT = (((1,), (1,)), ((), ()))
TN = (((0,), (0,)), ((), ()))


def _dot(a, b, dims=NN, prec=None):
    return lax.dot_general(a, b, dims, precision=prec, preferred_element_type=F32)


def _bdot(a, b, dims=NN):
    return _dot(a.astype(BF16), b.astype(BF16), dims)


def _sig(x):
    return 1.0 / (1.0 + jnp.exp(-x))


class _Carry:
    def __init__(self, ins, outs, aliases, scratch, phases):
        self.ins, self.outs, self.aliases, self.scratch, self.phases = ins, outs, aliases, scratch, phases


def _in_hbm(args):
    return [pltpu.with_memory_space_constraint(a, pltpu.HBM) for a in args]


def _out_hbm(shapes):
    if isinstance(shapes, (list, tuple)):
        return [pltpu.HBM(s.shape, s.dtype) for s in shapes]
    return pltpu.HBM(shapes.shape, shapes.dtype)


def _carry_join(a, b):
    na_in, na_out, na_sc = len(a.ins), len(a.outs), len(a.scratch)
    aliases = dict(a.aliases)
    aliases.update({na_in + i: na_out + o for i, o in b.aliases.items()})

    def phases(ins, outs, sems):
        pa = a.phases(ins[:na_in], outs[:na_out], sems[:na_sc])
        pb = b.phases(ins[na_in:], outs[na_out:], sems[na_sc:])

        def both(fa, fb):
            if fa is None and fb is None:
                return None

            def run():
                for fn in (fa, fb):
                    if fn is not None:
                        fn()
            return run

        return tuple(both(fa, fb) for fa, fb in zip(pa, pb))

    return _Carry(list(a.ins) + list(b.ins), list(a.outs) + list(b.outs), aliases,
                  list(a.scratch) + list(b.scratch), phases)


def _pcall(body, *, name, grid, in_specs, out_specs, out_shape, scratch=(), aliases=None, vmem_mb=48,
           carry=None):
    params = pltpu.CompilerParams(dimension_semantics=("arbitrary",) * len(grid),
                                  vmem_limit_bytes=vmem_mb << 20)
    if carry is None:
        plain = pl.pallas_call(
            body, name=name, grid=grid, in_specs=in_specs, out_specs=out_specs, out_shape=_out_hbm(out_shape),
            scratch_shapes=list(scratch), input_output_aliases=aliases or {}, compiler_params=params)
        return lambda *args: plain(*_in_hbm(args))
    single = not isinstance(out_shape, (list, tuple))
    out_specs_l = [out_specs] if single else list(out_specs)
    out_shape_l = [out_shape] if single else list(out_shape)
    n_in, n_out, n_sc = len(in_specs), len(out_shape_l), len(scratch)
    k_in, k_out = len(carry.ins), len(carry.outs)
    nsteps = int(np.prod(grid))
    assert nsteps >= 3

    def wrapped(*refs):
        ins, cins = refs[:n_in], refs[n_in:n_in + k_in]
        o0 = n_in + k_in
        outs, couts = refs[o0:o0 + n_out], refs[o0 + n_out:o0 + n_out + k_out]
        s0 = o0 + n_out + k_out
        sc, csc = refs[s0:s0 + n_sc], refs[s0 + n_sc:]
        step = pl.program_id(0)
        for ax in range(1, len(grid)):
            step = step * grid[ax] + pl.program_id(ax)
        start, mid, end = carry.phases(cins, couts, csc)
        pl.when(step == 0)(start)
        body(*ins, *outs, *sc)
        if mid is not None:
            pl.when(step == nsteps - 2)(mid)
        pl.when(step == nsteps - 1)(end)

    all_aliases = dict(aliases or {})
    all_aliases.update({n_in + i: n_out + o for i, o in carry.aliases.items()})
    call = pl.pallas_call(
        wrapped, name=name, grid=grid, in_specs=list(in_specs) + [ANY] * k_in,
        out_specs=out_specs_l + [ANY] * k_out, out_shape=_out_hbm(out_shape_l + list(carry.outs)),
        scratch_shapes=list(scratch) + list(carry.scratch), input_output_aliases=all_aliases,
        compiler_params=params)

    def run(*args):
        res = call(*_in_hbm(args), *carry.ins)
        core = res[:n_out]
        return (core[0] if single else list(core)), list(res[n_out:])

    return run


def _full(shape):
    nd = len(shape)
    return pl.BlockSpec(shape, lambda *_: (0,) * nd)


ANY = pl.BlockSpec(memory_space=pl.ANY)


def _mm(a, b, *, name, mode="nn", out_dtype=F32, tm, tn, tk):
    if mode == "nn":
        (m, k), (k2, n) = a.shape, b.shape
    elif mode == "nt":
        (m, k), (n, k2) = a.shape, b.shape
    else:
        (k, m), (k2, n) = a.shape, b.shape
    assert k == k2 and m % tm == 0 and n % tn == 0 and k % tk == 0, (name, a.shape, b.shape)
    nk = k // tk
    dims = {"nn": NN, "nt": NT, "tn": TN}[mode]

    def body(a_ref, b_ref, o_ref, acc):
        kk = pl.program_id(2)

        @pl.when(kk == 0)
        def _():
            acc[...] = jnp.zeros_like(acc)

        acc[...] += _bdot(a_ref[...], b_ref[...], dims)

        @pl.when(kk == nk - 1)
        def _():
            o_ref[...] = acc[...].astype(out_dtype)

    a_spec = (pl.BlockSpec((tk, tm), lambda i, j, kk: (kk, i)) if mode == "tn"
              else pl.BlockSpec((tm, tk), lambda i, j, kk: (i, kk)))
    b_spec = (pl.BlockSpec((tn, tk), lambda i, j, kk: (j, kk)) if mode == "nt"
              else pl.BlockSpec((tk, tn), lambda i, j, kk: (kk, j)))
    return _pcall(body, name=name, grid=(m // tm, n // tn, nk), in_specs=[a_spec, b_spec],
                  out_specs=pl.BlockSpec((tm, tn), lambda i, j, kk: (i, j)),
                  out_shape=jax.ShapeDtypeStruct((m, n), out_dtype),
                  scratch=[pltpu.VMEM((tm, tn), F32)])(a, b)


NT_IN = NCOL // 256


def _src_block(j):
    return j + jnp.where(j < 4, 2, jnp.where(j < 6, 3, jnp.where(j < 8, -6, jnp.where(
        j < 16, 5, jnp.where(j < 20, -7, -14)))))


def _mm_in(h, wt, tm, carry=None):
    tt = h.shape[0]

    def body(h_ref, w_ref, o_ref):
        o_ref[...] = _bdot(h_ref[...], w_ref[...], NT)

    return _pcall(body, name="mm_in", grid=(tt // tm, NT_IN),
                  in_specs=[pl.BlockSpec((tm, D), lambda i, j: (i, 0)),
                            pl.BlockSpec((256, D), lambda i, j: (_src_block(j), 0))],
                  out_specs=pl.BlockSpec((tm, 256), lambda i, j: (i, j)),
                  out_shape=jax.ShapeDtypeStruct((tt, NCOL), F32), carry=carry)(h, wt)


def _mm_dh(dp, wt, tm, carry=None):
    tt = dp.shape[0]
    per, ng = 3, NT_IN // 3

    def body(d_ref, w0, w1, w2, o_ref, acc):
        kk = pl.program_id(1)

        @pl.when(kk == 0)
        def _():
            acc[...] = jnp.zeros_like(acc)

        acc[...] += (_bdot(d_ref[:, 0:256], w0[...]) + _bdot(d_ref[:, 256:512], w1[...])
                     + _bdot(d_ref[:, 512:768], w2[...]))

        @pl.when(kk == ng - 1)
        def _():
            o_ref[...] = acc[...]

    wspecs = [pl.BlockSpec((256, D), functools.partial(lambda t, i, kk: (_src_block(per * kk + t), 0), t))
              for t in range(per)]
    return _pcall(body, name="mm_dh", grid=(tt // tm, ng),
                  in_specs=[pl.BlockSpec((tm, per * 256), lambda i, kk: (i, kk))] + wspecs,
                  out_specs=pl.BlockSpec((tm, D), lambda i, kk: (i, 0)),
                  out_shape=jax.ShapeDtypeStruct((tt, D), F32), scratch=[pltpu.VMEM((tm, D), F32)],
                  carry=carry)(dp, wt, wt, wt)


def _mm_gin(dp, h, tk):
    tt = dp.shape[0]
    nk = tt // tk

    def body(d_ref, h_ref, o_ref, acc):
        kk = pl.program_id(1)

        @pl.when(kk == 0)
        def _():
            acc[...] = jnp.zeros_like(acc)

        acc[...] += _bdot(d_ref[...], h_ref[...], TN)

        @pl.when(kk == nk - 1)
        def _():
            o_ref[...] = acc[...]

    return _pcall(body, name="mm_gin", grid=(NT_IN, nk),
                  in_specs=[pl.BlockSpec((tk, 256), lambda j, kk: (kk, j)),
                            pl.BlockSpec((tk, D), lambda j, kk: (kk, 0))],
                  out_specs=pl.BlockSpec((256, D), lambda j, kk: (_src_block(j), 0)),
                  out_shape=jax.ShapeDtypeStruct((NCOL, D), F32), scratch=[pltpu.VMEM((256, D), F32)])(dp, h)


def _modulate(xin, nw, ss, *, name, sel):
    rows = xin.shape[0]

    def body(x_ref, nw_ref, ss_ref, h_ref):
        x = x_ref[...]
        r = lax.rsqrt(jnp.mean(x * x, axis=-1, keepdims=True) + EPS)
        s = ss_ref[0]
        h_ref[...] = ((x * r * nw_ref[...]) * (1.0 + s[1:2]) + s[0:1]).astype(BF16)

    return _pcall(body, name=name, grid=(rows // TM,),
                  in_specs=[pl.BlockSpec((TM, D), lambda i: (i, 0)), _full((1, D)),
                            pl.BlockSpec((1, 2, D), lambda i: (sel(i), 0, 0))],
                  out_specs=pl.BlockSpec((TM, D), lambda i: (i, 0)),
                  out_shape=jax.ShapeDtypeStruct((rows, D), BF16))(xin, nw, ss)


def _norm_bwd_rows(x, dh, nw, scale):
    r = lax.rsqrt(jnp.mean(x * x, axis=-1, keepdims=True) + EPS)
    xh = x * r
    dxh = dh * ((1.0 + scale) * nw)
    dx = r * (dxh - xh * jnp.mean(dxh * xh, axis=-1, keepdims=True))
    return dx, xh


def _res1_mod2(x, ao, g1, nw2, ss2):
    s_len = x.shape[0]

    def body(x_ref, ao_ref, g_ref, nw_ref, ss_ref, x1_ref, h_ref):
        x1 = x_ref[...] + g_ref[...] * ao_ref[...]
        x1_ref[...] = x1
        r = lax.rsqrt(jnp.mean(x1 * x1, axis=-1, keepdims=True) + EPS)
        s = ss_ref[0]
        h_ref[...] = ((x1 * r * nw_ref[...]) * (1.0 + s[1:2]) + s[0:1]).astype(BF16)

    row = pl.BlockSpec((TM, D), lambda i: (i, 0))
    return _pcall(body, name="res1_mod2", grid=(s_len // TM,),
                  in_specs=[row, row, _full((1, D)), _full((1, D)), _full((1, 2, D))],
                  out_specs=[row, row],
                  out_shape=[jax.ShapeDtypeStruct((s_len, D), F32),
                             jax.ShapeDtypeStruct((s_len, D), BF16)])(x, ao, g1, nw2, ss2)


TS = 1024


def _acc_call(body, *, name, grid, in_specs, out_specs, out_shape, acc_shapes, args, carry=None):
    return _pcall(body, name=name, grid=grid, in_specs=in_specs, out_specs=out_specs, out_shape=out_shape,
                  scratch=[pltpu.VMEM(s, F32) for s in acc_shapes], carry=carry)(*args)


def _mm_cs(a, w4, *, name):
    m, k = a.shape
    _, _, ns = w4.shape

    def body(a_ref, w_ref, o_ref):
        o_ref[...] = _bdot(a_ref[...], w_ref[0])

    return _pcall(body, name=name, grid=(m // TS, 4),
                  in_specs=[pl.BlockSpec((TS, k), lambda i, j: (i, 0)),
                            pl.BlockSpec((1, k, ns), lambda i, j: (j, 0, 0))],
                  out_specs=pl.BlockSpec((TS, ns), lambda i, j: (i, j)),
                  out_shape=jax.ShapeDtypeStruct((m, 4 * ns), F32))(a, w4)


def _mm_cs_nt(a, w4, *, name):
    m = a.shape[0]
    _, k, ns = w4.shape

    def body(a_ref, w_ref, o_ref, acc):
        j = pl.program_id(1)

        @pl.when(j == 0)
        def _():
            acc[...] = jnp.zeros_like(acc)

        acc[...] += _bdot(a_ref[...], w_ref[0], NT)

        @pl.when(j == 3)
        def _():
            o_ref[...] = acc[...]

    return _acc_call(body, name=name, grid=(m // TS, 4),
                     in_specs=[pl.BlockSpec((TS, ns), lambda i, j: (i, j)),
                               pl.BlockSpec((1, k, ns), lambda i, j: (j, 0, 0))],
                     out_specs=pl.BlockSpec((TS, k), lambda i, j: (i, 0)),
                     out_shape=jax.ShapeDtypeStruct((m, k), F32), acc_shapes=[(TS, k)], args=(a, w4))


def _mm_cs_tn(a, b, ns, *, name):
    s_len, k = a.shape
    nk = s_len // TS

    def body(a_ref, b_ref, o_ref, acc):
        t = pl.program_id(1)

        @pl.when(t == 0)
        def _():
            acc[...] = jnp.zeros_like(acc)

        acc[...] += _bdot(a_ref[...], b_ref[...], TN)

        @pl.when(t == nk - 1)
        def _():
            o_ref[0] = acc[...]

    return _acc_call(body, name=name, grid=(4, nk),
                     in_specs=[pl.BlockSpec((TS, k), lambda j, t: (t, 0)),
                               pl.BlockSpec((TS, ns), lambda j, t: (t, j))],
                     out_specs=pl.BlockSpec((1, k, ns), lambda j, t: (j, 0, 0)),
                     out_shape=jax.ShapeDtypeStruct((4, k, ns), F32), acc_shapes=[(k, ns)], args=(a, b))


def _ffn_up(h2, g4, u4, carry=None):
    s_len = h2.shape[0]
    ns = g4.shape[1]

    def body(h_ref, g_ref, u_ref, a_ref, b_ref, z_ref):
        h = h_ref[...]
        a = _bdot(h, g_ref[0], NT)
        b = _bdot(h, u_ref[0], NT)
        a_ref[0] = a.astype(BF16)
        b_ref[0] = b.astype(BF16)
        z_ref[0] = (a * _sig(a) * b).astype(BF16)

    w = pl.BlockSpec((1, ns, D), lambda i, j: (j, 0, 0))
    o = pl.BlockSpec((1, TS, ns), lambda i, j: (j, i, 0))
    f = jax.ShapeDtypeStruct((4, s_len, ns), BF16)
    return _pcall(body, name="ffn_up", grid=(s_len // TS, 4),
                  in_specs=[pl.BlockSpec((TS, D), lambda i, j: (i, 0)), w, w], out_specs=[o, o, o],
                  out_shape=[f, f, jax.ShapeDtypeStruct((4, s_len, ns), BF16)], carry=carry)(h2, g4, u4)


def _ffn_down(z4, dn4):
    _, s_len, ns = z4.shape

    def body(z_ref, w_ref, o_ref, acc):
        j = pl.program_id(1)

        @pl.when(j == 0)
        def _():
            acc[...] = jnp.zeros_like(acc)

        acc[...] += _bdot(z_ref[0], w_ref[0])

        @pl.when(j == 3)
        def _():
            o_ref[...] = acc[...]

    return _acc_call(body, name="ffn_down", grid=(s_len // TS, 4),
                     in_specs=[pl.BlockSpec((1, TS, ns), lambda i, j: (j, i, 0)),
                               pl.BlockSpec((1, ns, D), lambda i, j: (j, 0, 0))],
                     out_specs=pl.BlockSpec((TS, D), lambda i, j: (i, 0)),
                     out_shape=jax.ShapeDtypeStruct((s_len, D), F32), acc_shapes=[(TS, D)], args=(z4, dn4))


def _ffn_dz(dyb, dn4, a4, b4):
    _, s_len, ns = a4.shape

    def body(dy_ref, w_ref, a_ref, b_ref, da_ref, db_ref):
        dz = _bdot(dy_ref[...], w_ref[0], NT)
        a = a_ref[0].astype(F32)
        s = _sig(a)
        da_ref[0] = (dz * b_ref[0].astype(F32) * (s * (1.0 + a * (1.0 - s)))).astype(BF16)
        db_ref[0] = (dz * (a * s)).astype(BF16)

    t = pl.BlockSpec((1, TS, ns), lambda i, j: (j, i, 0))
    o = jax.ShapeDtypeStruct((4, s_len, ns), BF16)
    return _pcall(body, name="ffn_dz", grid=(s_len // TS, 4),
                  in_specs=[pl.BlockSpec((TS, D), lambda i, j: (i, 0)),
                            pl.BlockSpec((1, ns, D), lambda i, j: (j, 0, 0)), t, t],
                  out_specs=[t, t], out_shape=[o, o])(dyb, dn4, a4, b4)


def _ffn_gdn(z4, dyb):
    _, s_len, ns = z4.shape
    nk = s_len // TS

    def body(z_ref, dy_ref, o_ref, acc):
        t = pl.program_id(1)

        @pl.when(t == 0)
        def _():
            acc[...] = jnp.zeros_like(acc)

        acc[...] += _bdot(z_ref[0], dy_ref[...], TN)

        @pl.when(t == nk - 1)
        def _():
            o_ref[0] = acc[...]

    return _acc_call(body, name="ffn_gdn", grid=(4, nk),
                     in_specs=[pl.BlockSpec((1, TS, ns), lambda j, t: (j, t, 0)),
                               pl.BlockSpec((TS, D), lambda j, t: (t, 0))],
                     out_specs=pl.BlockSpec((1, ns, D), lambda j, t: (j, 0, 0)),
                     out_shape=jax.ShapeDtypeStruct((4, ns, D), F32), acc_shapes=[(ns, D)], args=(z4, dyb))


def _ffn_dh2(da4, db4, g4, u4, carry=None):
    _, s_len, ns = da4.shape

    def body(da_ref, db_ref, g_ref, u_ref, o_ref, acc):
        j = pl.program_id(1)

        @pl.when(j == 0)
        def _():
            acc[...] = jnp.zeros_like(acc)

        acc[...] += _bdot(da_ref[0], g_ref[0]) + _bdot(db_ref[0], u_ref[0])

        @pl.when(j == 3)
        def _():
            o_ref[...] = acc[...]

    t = pl.BlockSpec((1, TS, ns), lambda i, j: (j, i, 0))
    w = pl.BlockSpec((1, ns, D), lambda i, j: (j, 0, 0))
    return _acc_call(body, name="ffn_dh2", grid=(s_len // TS, 4), in_specs=[t, t, w, w],
                     out_specs=pl.BlockSpec((TS, D), lambda i, j: (i, 0)),
                     out_shape=jax.ShapeDtypeStruct((s_len, D), F32), acc_shapes=[(TS, D)],
                     args=(da4, db4, g4, u4), carry=carry)


def _ffn_ggu(h2, da4, db4, carry=None):
    _, s_len, ns = da4.shape
    nk = s_len // TS

    def body(h_ref, da_ref, db_ref, gg_ref, gu_ref, acc_g, acc_u):
        t = pl.program_id(1)

        @pl.when(t == 0)
        def _():
            acc_g[...] = jnp.zeros_like(acc_g)
            acc_u[...] = jnp.zeros_like(acc_u)

        h = h_ref[...]
        acc_g[...] += _bdot(da_ref[0], h, TN)
        acc_u[...] += _bdot(db_ref[0], h, TN)

        @pl.when(t == nk - 1)
        def _():
            gg_ref[0] = acc_g[...]
            gu_ref[0] = acc_u[...]

    d = pl.BlockSpec((1, TS, ns), lambda j, t: (j, t, 0))
    o = pl.BlockSpec((1, ns, D), lambda j, t: (j, 0, 0))
    f = jax.ShapeDtypeStruct((4, ns, D), F32)
    return _acc_call(body, name="ffn_ggu", grid=(4, nk),
                     in_specs=[pl.BlockSpec((TS, D), lambda j, t: (t, 0)), d, d], out_specs=[o, o],
                     out_shape=[f, f], acc_shapes=[(ns, D), (ns, D)], args=(h2, da4, db4), carry=carry)


def _loss_head(x1, y, g2, tgt):
    s_len = x1.shape[0]

    def body(x1_ref, y_ref, g_ref, t_ref, sq_ref, dx2_ref, dyb_ref, dg_ref):
        i = pl.program_id(0)

        @pl.when(i == 0)
        def _():
            sq_ref[...] = jnp.zeros_like(sq_ref)
            dg_ref[...] = jnp.zeros_like(dg_ref)

        y_ = y_ref[...]
        g = g_ref[...]
        e = x1_ref[...] + g * y_ - t_ref[...]
        sq_ref[...] += jnp.sum(e * e, axis=0, keepdims=True)
        dx2 = e * (1.0 / D)
        dx2_ref[...] = dx2
        dyb_ref[...] = (g * dx2).astype(BF16)
        dg_ref[...] += jnp.sum(dx2 * y_, axis=0, keepdims=True)

    row = pl.BlockSpec((TM, D), lambda i: (i, 0))
    vec = _full((1, D))
    return _pcall(body, name="loss_head", grid=(s_len // TM,),
                  in_specs=[row, row, vec, row], out_specs=[vec, row, row, vec],
                  out_shape=[jax.ShapeDtypeStruct((1, D), F32), jax.ShapeDtypeStruct((s_len, D), F32),
                             jax.ShapeDtypeStruct((s_len, D), BF16),
                             jax.ShapeDtypeStruct((1, D), F32)])(x1, y, g2, tgt)


def _mod2_bwd(x1, dh2, dx2, ao, nw2, ss2, g1):
    s_len = x1.shape[0]

    def body(x1_ref, dh_ref, dx2_ref, ao_ref, nw_ref, ss_ref, g_ref,
             dx1_ref, da_ref, dss_ref, dnw_ref, dg_ref):
        i = pl.program_id(0)

        @pl.when(i == 0)
        def _():
            dss_ref[...] = jnp.zeros_like(dss_ref)
            dnw_ref[...] = jnp.zeros_like(dnw_ref)
            dg_ref[...] = jnp.zeros_like(dg_ref)

        dh = dh_ref[...]
        nw = nw_ref[...]
        scale = ss_ref[0][1:2]
        dxn, xh = _norm_bwd_rows(x1_ref[...], dh, nw, scale)
        dx1 = dx2_ref[...] + dxn
        dx1_ref[...] = dx1
        da_ref[...] = (g_ref[...] * dx1).astype(BF16)
        dg_ref[...] += jnp.sum(dx1 * ao_ref[...], axis=0, keepdims=True)
        dsh = jnp.sum(dh, axis=0, keepdims=True)
        dsc = jnp.sum(dh * xh * nw, axis=0, keepdims=True)
        dss_ref[...] += jnp.concatenate([dsh, dsc], axis=0)
        dnw_ref[...] += jnp.sum(dh * xh * (1.0 + scale), axis=0, keepdims=True)

    row = pl.BlockSpec((TM, D), lambda i: (i, 0))
    vec = _full((1, D))
    return _pcall(body, name="mod2_bwd", grid=(s_len // TM,),
                  in_specs=[row, row, row, row, vec, _full((1, 2, D)), vec],
                  out_specs=[row, row, _full((2, D)), vec, vec],
                  out_shape=[jax.ShapeDtypeStruct((s_len, D), F32), jax.ShapeDtypeStruct((s_len, D), BF16),
                             jax.ShapeDtypeStruct((2, D), F32), jax.ShapeDtypeStruct((1, D), F32),
                             jax.ShapeDtypeStruct((1, D), F32)])(x1, dh2, dx2, ao, nw2, ss2, g1)


def _mod1_bwd(tok, dh, dx1, nw1, ss1, carry=None):
    tt = tok.shape[0]
    s_len = dx1.shape[0]

    def body(t_ref, dh_ref, dx1_ref, nw_ref, ss_ref, dx_ref, dss_ref, dnw_ref):
        i = pl.program_id(0)

        @pl.when(i == 0)
        def _():
            dnw_ref[...] = jnp.zeros_like(dnw_ref)

        @pl.when(i <= 1)
        def _():
            dss_ref[...] = jnp.zeros_like(dss_ref)

        dh_ = dh_ref[...]
        nw = nw_ref[...]
        scale = ss_ref[0][1:2]
        dxn, xh = _norm_bwd_rows(t_ref[...], dh_, nw, scale)

        @pl.when(i >= 1)
        def _():
            dx_ref[...] = dx1_ref[...] + dxn

        dsh = jnp.sum(dh_, axis=0, keepdims=True)
        dsc = jnp.sum(dh_ * xh * nw, axis=0, keepdims=True)
        dss_ref[...] += jnp.concatenate([dsh, dsc], axis=0)[None]
        dnw_ref[...] += jnp.sum(dh_ * xh * (1.0 + scale), axis=0, keepdims=True)

    row = pl.BlockSpec((TM, D), lambda i: (i, 0))
    lat = pl.BlockSpec((TM, D), lambda i: (jnp.maximum(i - 1, 0), 0))
    sel = pl.BlockSpec((1, 2, D), lambda i: (jnp.minimum(i, 1), 0, 0))
    return _pcall(body, name="mod1_bwd", grid=(tt // TM,),
                  in_specs=[row, row, lat, _full((1, D)), sel],
                  out_specs=[lat, sel, _full((1, D))],
                  out_shape=[jax.ShapeDtypeStruct((s_len, D), F32), jax.ShapeDtypeStruct((2, 2, D), F32),
                             jax.ShapeDtypeStruct((1, D), F32)], carry=carry)(tok, dh, dx1, nw1, ss1)


def _rows(c):
    return slice(c * CH, (c + 1) * CH)


def _chunk_masks(rev, transpose=False):
    r = lax.broadcasted_iota(jnp.int32, (TM, TM), 0)
    c = lax.broadcasted_iota(jnp.int32, (TM, TM), 1)
    same = (r // CH) == (c // CH)
    before = (c >= r) if (rev != transpose) else (c <= r)
    return same & before, same


def _chunk_scan(x, rev, transpose=False):
    r = lax.broadcasted_iota(jnp.int32, (CH, CH), 0)
    c = lax.broadcasted_iota(jnp.int32, (CH, CH), 1)
    tri = ((c >= r) if (rev != transpose) else (c <= r)).astype(F32)
    return jnp.concatenate([_dot(tri, x[_rows(ch)], prec=HI) for ch in range(x.shape[0] // CH)], axis=0)


def _chunk_total(x):
    return jnp.concatenate([jnp.broadcast_to(jnp.sum(x[_rows(ch)], axis=0, keepdims=True), (CH, x.shape[1]))
                            for ch in range(x.shape[0] // CH)], axis=0)


def _hgrn_gate(fl, qraw, lg):
    lb = 1.0 / (1.0 + jnp.exp(lg[1:2] - lg[0:1]))
    sg = _sig(fl)
    f = lb + (1.0 - lb) * sg
    q = qraw * _sig(qraw) * (HGD ** -0.5)
    return lb, sg, f, q


def _hgrn_fwd(p, lg, *, rev, carry=None):
    tt = p.shape[0]
    nt = tt // TM
    ncht = TM // CH
    d = 1 if rev else 0

    def tile_of(s):
        return jnp.where(s == 0, 0, nt - s) if rev else s

    def body(f_ref, inp_ref, q_ref, lg_ref, o_ref, st_ref, state):
        s = pl.program_id(0)

        @pl.when(s == 0)
        def _():
            state[...] = jnp.zeros_like(state)

        _, _, f, q = _hgrn_gate(f_ref[...], q_ref[...], lg_ref[0])
        lf = jnp.log(f)
        causal, _ = _chunk_masks(rev)
        cum = _chunk_scan(lf, rev)
        tot = _chunk_total(lf)
        qd = (q * jnp.exp(cum)).astype(BF16)
        kd = ((1.0 - f) * jnp.exp(-cum)).astype(BF16)
        ke = ((1.0 - f) * jnp.exp(tot - cum)).astype(BF16)
        et = jnp.exp(tot)
        v = inp_ref[...].astype(BF16)
        order = range(ncht - 1, -1, -1) if rev else range(ncht)
        outs = []
        for h in range(4):
            sl = slice(h * HGD, (h + 1) * HGD)
            qd_, kd_, ke_, v_ = qd[:, sl], kd[:, sl], ke[:, sl], v[:, sl]
            pm = jnp.where(causal, _dot(qd_, kd_, NT), 0.0).astype(BF16)
            o_h = _dot(pm, v_)
            upd = [_dot(v_[_rows(c)], ke_[_rows(c)], TN) for c in range(ncht)]
            st = state[h]
            for c in order:
                st_ref[c, h] = st
                st = st * et[c * CH:c * CH + 1, sl] + upd[c]
            state[h] = st
            inter = [_dot(qd_[_rows(c)], st_ref[c, h].astype(BF16), NT) for c in range(ncht)]
            outs.append(o_h + jnp.concatenate(inter, axis=0))
        o_ref[...] = jnp.concatenate(outs, axis=1)

    def col(cb):
        return pl.BlockSpec((TM, HGW), lambda s: (tile_of(s), cb))

    return _pcall(
        body, name="hgrn_fwd_rev" if rev else "hgrn_fwd", grid=(nt,),
        in_specs=[col(C_FB if rev else C_FF), col(C_INP), col(C_QHG),
                  pl.BlockSpec((1, 2, HGW), lambda s: (d, 0, 0))],
        out_specs=[pl.BlockSpec((TM, HGW), lambda s: (tile_of(s), 0)),
                   pl.BlockSpec((ncht, 4, HGD, HGD), lambda s: (tile_of(s), 0, 0, 0))],
        out_shape=[jax.ShapeDtypeStruct((tt, HGW), F32),
                   jax.ShapeDtypeStruct((nt * ncht, 4, HGD, HGD), F32)],
        scratch=[pltpu.VMEM((4, HGD, HGD), F32)], carry=carry)(p, p, p, lg)


def _hgrn_bwd(p, lg, do, st, dp, prev, *, rev, carry=None):
    tt = p.shape[0]
    nt = tt // TM
    ncht = TM // CH
    d = 1 if rev else 0
    second = prev is not None

    def tile_of(s):
        return jnp.where(s == nt - 1, 0, s + 1) if rev else nt - 1 - s

    def body(*refs):
        if second:
            (f_ref, inp_ref, q_ref, lg_ref, do_ref, st_ref, dvp_ref, dqp_ref, _dp_in,
             dp_ref, dlg_ref, dstate) = refs
        else:
            (f_ref, inp_ref, q_ref, lg_ref, do_ref, st_ref, _dp_in,
             dp_ref, dv_ref, dq_ref, dlg_ref, dstate) = refs
        s = pl.program_id(0)
        tile = tile_of(s)

        @pl.when(s == 0)
        def _():
            dstate[...] = jnp.zeros_like(dstate)
            dlg_ref[...] = jnp.zeros_like(dlg_ref)

        qraw = q_ref[...]
        lb, sg, f, q = _hgrn_gate(f_ref[...], qraw, lg_ref[0])
        lf = jnp.log(f)
        causal, _ = _chunk_masks(rev)
        causal_t, _ = _chunk_masks(rev, transpose=True)
        cum = _chunk_scan(lf, rev)
        tot = _chunk_total(lf)
        ea, eb, ee, et = jnp.exp(cum), jnp.exp(-cum), jnp.exp(tot - cum), jnp.exp(tot)
        qdf, kdf, kef = q * ea, (1.0 - f) * eb, (1.0 - f) * ee
        qd, kd, ke = qdf.astype(BF16), kdf.astype(BF16), kef.astype(BF16)
        v = inp_ref[...].astype(BF16)
        dob = jnp.where(tile == 0, 0.0, do_ref[...]).astype(BF16)
        order = range(ncht) if rev else range(ncht - 1, -1, -1)
        dq_l, dk_l, dv_l, dcum_l, dtot_l = [], [], [], [], []
        for h in range(4):
            sl = slice(h * HGD, (h + 1) * HGD)
            qd_, kd_, ke_, v_, do_ = qd[:, sl], kd[:, sl], ke[:, sl], v[:, sl], dob[:, sl]
            pmt = jnp.where(causal_t, _dot(kd_, qd_, NT), 0.0).astype(BF16)
            dpm = jnp.where(causal, _dot(do_, v_, NT), 0.0).astype(BF16)
            dpmt = jnp.where(causal_t, _dot(v_, do_, NT), 0.0).astype(BF16)
            dv = _dot(pmt, do_)
            dqd = _dot(dpm, kd_)
            dkd = _dot(dpmt, qd_)
            upd = [_dot(do_[_rows(c)], qd_[_rows(c)], TN) for c in range(ncht)]
            ds = dstate[h]
            ds1 = [None] * ncht
            for c in order:
                ds1[c] = ds
                ds = ds * et[c * CH:c * CH + 1, sl] + upd[c]
            dstate[h] = ds
            dke_c, dv_c, dqd_c, dtot_c = [], [], [], []
            for c in range(ncht):
                st0 = st_ref[c, h]
                dsb = ds1[c].astype(BF16)
                dke_ = _dot(v_[_rows(c)], dsb)
                dke_c.append(dke_)
                dv_c.append(_dot(ke_[_rows(c)], dsb, NT))
                dqd_c.append(_dot(do_[_rows(c)], st0.astype(BF16)))
                dt = (jnp.sum(ds1[c] * st0, axis=0, keepdims=True) * et[c * CH:c * CH + 1, sl]
                      + jnp.sum(dke_ * kef[_rows(c), sl], axis=0, keepdims=True))
                dtot_c.append(jnp.broadcast_to(dt, (CH, HGD)))
            dke = jnp.concatenate(dke_c, axis=0)
            dqd = dqd + jnp.concatenate(dqd_c, axis=0)
            dv_l.append(dv + jnp.concatenate(dv_c, axis=0))
            dtot_l.append(jnp.concatenate(dtot_c, axis=0))
            dq_l.append(dqd * ea[:, sl])
            dk_l.append(dkd * eb[:, sl] + dke * ee[:, sl])
            dcum_l.append(dqd * qdf[:, sl] - dkd * kdf[:, sl] - dke * kef[:, sl])
        dcum = jnp.concatenate(dcum_l, axis=1)
        dlf = _chunk_scan(dcum, rev, transpose=True) + jnp.concatenate(dtot_l, axis=1)
        dq_t = jnp.concatenate(dq_l, axis=1)
        dv_t = jnp.concatenate(dv_l, axis=1)

        df = dlf / f - jnp.concatenate(dk_l, axis=1)
        dfl = df * (1.0 - lb) * sg * (1.0 - sg)
        dlb = jnp.sum(df * (1.0 - sg), axis=0, keepdims=True)
        dl0 = dlb * lb * (1.0 - lb)
        dlg_ref[...] += jnp.concatenate([dl0, -dl0], axis=0)[None]
        if second:
            sq = _sig(qraw)
            dqr = (dqp_ref[...] + dq_t) * (HGD ** -0.5) * (sq * (1.0 + qraw * (1.0 - sq)))
            dp_ref[...] = jnp.concatenate([dfl, dvp_ref[...] + dv_t, dqr], axis=1).astype(BF16)
        else:
            dp_ref[...] = dfl.astype(BF16)
            dv_ref[...] = dv_t
            dq_ref[...] = dq_t

    def col(cb):
        return pl.BlockSpec((TM, HGW), lambda s: (tile_of(s), cb))

    tok = pl.BlockSpec((TM, HGW), lambda s: (tile_of(s), 0))
    in_specs = [col(C_FB if rev else C_FF), col(C_INP), col(C_QHG),
                pl.BlockSpec((1, 2, HGW), lambda s: (d, 0, 0)),
                pl.BlockSpec((TM, HGW), lambda s: (jnp.maximum(tile_of(s) - 1, 0), 0)),
                pl.BlockSpec((ncht, 4, HGD, HGD), lambda s: (tile_of(s), 0, 0, 0))]
    args = [p, p, p, lg, do, st]
    dlg_spec = _full((1, 2, HGW))
    dlg_shape = jax.ShapeDtypeStruct((1, 2, HGW), F32)
    if second:
        in_specs += [tok, tok]
        args += [prev[0], prev[1]]
        out_specs = [pl.BlockSpec((TM, 3 * HGW), lambda s: (tile_of(s), 0)), dlg_spec]
        out_shape = [jax.ShapeDtypeStruct(dp.shape, BF16), dlg_shape]
    else:
        out_specs = [pl.BlockSpec((TM, HGW), lambda s: (tile_of(s), C_FB if rev else C_FF)), tok, tok, dlg_spec]
        out_shape = [jax.ShapeDtypeStruct(dp.shape, BF16), jax.ShapeDtypeStruct((tt, HGW), F32),
                     jax.ShapeDtypeStruct((tt, HGW), F32), dlg_shape]
    in_specs.append(ANY)
    args.append(dp)
    return _pcall(body, name="hgrn_bwd_rev" if rev else "hgrn_bwd", grid=(nt,),
                  in_specs=in_specs, out_specs=out_specs, out_shape=out_shape,
                  scratch=[pltpu.VMEM((4, HGD, HGD), F32)],
                  aliases={len(args) - 1: 0}, carry=carry)(*args)


def _head_rms(o, w, nheads):
    outs = []
    for h in range(nheads):
        oh = o[:, h * HGD:(h + 1) * HGD]
        outs.append(oh * lax.rsqrt(jnp.mean(oh * oh, axis=-1, keepdims=True) + EPS))
    return jnp.concatenate(outs, axis=1)


def _readout(o0, o1, p, hw4):
    s_len = o0.shape[0] - L

    def body(o0_ref, o1_ref, g_ref, w_ref, y_ref):
        xh = _head_rms(o0_ref[...] + o1_ref[...], None, 4)
        g = g_ref[...]
        y_ref[...] = (xh * w_ref[...] * (g * _sig(g))).astype(BF16)

    lat = pl.BlockSpec((TM, HGW), lambda i: (i + 1, 0))
    return _pcall(body, name="readout", grid=(s_len // TM,),
                  in_specs=[lat, lat, pl.BlockSpec((TM, HGW), lambda i: (i + 1, C_GHG)), _full((1, HGW))],
                  out_specs=pl.BlockSpec((TM, HGW), lambda i: (i, 0)),
                  out_shape=jax.ShapeDtypeStruct((s_len, HGW), BF16))(o0, o1, p, hw4)


def _readout_bwd(o0, o1, p, hw4, dy, dp, carry=None):
    tt = o0.shape[0]
    s_len = tt - L

    def body(o0_ref, o1_ref, g_ref, w_ref, dy_ref, _dp_in, dp_ref, do_ref, dw_ref):
        i = pl.program_id(0)

        @pl.when(i == 0)
        def _():
            dw_ref[...] = jnp.zeros_like(dw_ref)
            dp_ref[...] = jnp.zeros_like(dp_ref)

        @pl.when(i >= 1)
        def _():
            o = o0_ref[...] + o1_ref[...]
            g = g_ref[...]
            w = w_ref[...]
            sg = _sig(g)
            dy_ = dy_ref[...]
            dsw = dy_ * (g * sg)
            outs, xhs = [], []
            for h in range(4):
                sl = slice(h * HGD, (h + 1) * HGD)
                oh = o[:, sl]
                r = lax.rsqrt(jnp.mean(oh * oh, axis=-1, keepdims=True) + EPS)
                xh = oh * r
                dxh = dsw[:, sl] * w[:, sl]
                outs.append(r * (dxh - xh * jnp.mean(dxh * xh, axis=-1, keepdims=True)))
                xhs.append(xh)
            xh = jnp.concatenate(xhs, axis=1)
            do_ref[...] = jnp.concatenate(outs, axis=1)
            dp_ref[...] = (dy_ * xh * w * (sg * (1.0 + g * (1.0 - sg)))).astype(BF16)
            dw_ref[...] += jnp.sum(dsw * xh, axis=0, keepdims=True)

    tok = pl.BlockSpec((TM, HGW), lambda i: (i, 0))
    lat = pl.BlockSpec((TM, HGW), lambda i: (jnp.maximum(i - 1, 0), 0))
    return _pcall(body, name="readout_bwd", grid=(tt // TM,),
                  in_specs=[tok, tok, pl.BlockSpec((TM, HGW), lambda i: (i, C_GHG)), _full((1, HGW)), lat, ANY],
                  out_specs=[pl.BlockSpec((TM, HGW), lambda i: (i, C_GHG)), lat, _full((1, HGW))],
                  out_shape=[jax.ShapeDtypeStruct(dp.shape, BF16), jax.ShapeDtypeStruct((s_len, HGW), F32),
                             jax.ShapeDtypeStruct((1, HGW), F32)],
                  aliases={5: 0}, carry=carry)(o0, o1, p, hw4, dy, dp)


def _rope_tables(s_len):
    t = np.arange(s_len)
    inv = ROPE_THETA ** (-np.arange(0, 32, 2, dtype=np.float64) / 32)
    def half(pos):
        ang = pos[:, None].astype(np.float64) * inv[None, :]
        return (np.concatenate([np.cos(ang), np.cos(ang)], 1), np.concatenate([-np.sin(ang), np.sin(ang)], 1))
    cr, sr = half(t // GRID_W)
    cc, sc = half(t % GRID_W)
    cos = np.concatenate([cr, cc, cr, cc], 1)
    sin = np.concatenate([sr, sc, sr, sc], 1)
    cos = np.concatenate([np.ones((L, 128)), cos], 0)
    sin = np.concatenate([np.zeros((L, 128)), sin], 0)
    return jnp.asarray(cos, F32), jnp.asarray(sin, F32)


def _blockdiag(n, w):
    i = np.arange(n)
    return jnp.asarray((i[:, None] // w == i[None, :] // w) / float(w), F32)


def _dup_matrix():
    m = np.zeros((128, 512), np.float32)
    for g in range(2):
        for j in range(4):
            for dd in range(HDIM):
                m[64 * g + dd, 256 * g + 64 * j + dd] = 1.0
    return m


def _head_mean(x, blockdiag):
    return _dot(x, blockdiag, prec=lax.Precision.HIGH)


def _rot(x):
    n = x.shape[1]
    lane = lax.broadcasted_iota(jnp.int32, x.shape, 1)
    return jnp.where((lane % 32) < 16, pltpu.roll(x, n - 16, 1), pltpu.roll(x, 16, 1))


def _qk_prep(p, cos, sin, qnw8, knw2, bd512, bd128, dup):
    tt = p.shape[0]

    def body(q_ref, kv_ref, cos_ref, sin_ref, qw_ref, kw_ref, b5_ref, b1_ref, dup_ref,
             qr_ref, k4_ref, v4_ref):
        cos_, sin_ = cos_ref[...], sin_ref[...]
        q = q_ref[...]
        qn = q * lax.rsqrt(_head_mean(q * q, b5_ref[...]) + EPS) * qw_ref[...]
        cos4 = jnp.concatenate([cos_] * 4, axis=1)
        sin4 = jnp.concatenate([sin_] * 4, axis=1)
        qr_ref[...] = ((qn * cos4 + _rot(qn) * sin4) * (HDIM ** -0.5)).astype(BF16)
        kv = kv_ref[...]
        k, v = kv[:, :128], kv[:, 128:]
        kn = k * lax.rsqrt(_head_mean(k * k, b1_ref[...]) + EPS) * kw_ref[...]
        kr = kn * cos_ + _rot(kn) * sin_
        k4_ref[...] = _bdot(kr, dup_ref[...]).astype(BF16)
        v4_ref[...] = _bdot(v, dup_ref[...]).astype(BF16)

    row = lambda w, cb: pl.BlockSpec((TM, w), lambda i: (i, cb))
    out = jax.ShapeDtypeStruct((tt, ATW), BF16)
    return _pcall(body, name="qk_prep", grid=(tt // TM,),
                  in_specs=[row(ATW, C_QRAW), row(256, C_KV), row(128, 0), row(128, 0),
                            _full((1, ATW)), _full((1, 128)), _full((ATW, ATW)), _full((128, 128)),
                            _full((128, ATW))],
                  out_specs=[row(ATW, 0)] * 3, out_shape=[out] * 3)(
                      p, p, cos, sin, qnw8, knw2, bd512, bd128, dup)


def _attn_masks(i, nb):
    r = lax.broadcasted_iota(jnp.int32, (4 * BLK, 3 * BLK + L), 0) % BLK
    c = lax.broadcasted_iota(jnp.int32, (4 * BLK, 3 * BLK + L), 1)
    kpos = (i - 1) * BLK + c
    loc = (jnp.abs(c - BLK - r) <= BLK) & (kpos >= 0) & (kpos < nb * BLK)
    return loc | (c >= 3 * BLK)


def _stack_mask():
    r = lax.broadcasted_iota(jnp.int32, (4 * BLK, 256), 0)
    lane = lax.broadcasted_iota(jnp.int32, (4 * BLK, 256), 1)
    return (r // BLK) == (lane // HDIM)


def _stack_heads(xg, fill=0.0):
    x4 = jnp.concatenate([xg] * 4, axis=0)
    return jnp.where(_stack_mask(), x4, jnp.full_like(x4, fill))


def _unstack_heads(x4):
    out = jnp.where(_lane_mask(0), x4[0:BLK], 0.0)
    for j in range(1, 4):
        out = out + jnp.where(_lane_mask(j), x4[j * BLK:(j + 1) * BLK], 0.0)
    return out


def _per_head_rows(vals):
    return jnp.concatenate([jnp.broadcast_to(v, (BLK, 1)) for v in vals], axis=0)


def _lane_mask(j):
    lane = lax.broadcasted_iota(jnp.int32, (1, 256), 1)
    return (lane // HDIM) == j


def _attn_specs(nb):
    blk = lambda off: pl.BlockSpec((BLK, ATW), lambda i: (jnp.clip(i + off, 0, nb - 1) + 2, 0))
    ctx = pl.BlockSpec((L, ATW), lambda i: (0, 0))
    return blk, ctx


def _attn_fwd(qr, k4, v4, sinks, carry=None):
    tt = qr.shape[0]
    s_len = tt - L
    nb = s_len // BLK

    def body(sk_ref, q_ref, kp, ko, kn, kc, vp, vo, vn, vc, y_ref, lse_ref):
        i = pl.program_id(0)
        valid = _attn_masks(i, nb)
        q = q_ref[...]
        ys, lses = [], []
        for g in range(2):
            gs = slice(256 * g, 256 * g + 256)
            kcat = jnp.concatenate([kp[:, gs], ko[:, gs], kn[:, gs], kc[:, gs]], axis=0)
            vcat = jnp.concatenate([vp[:, gs], vo[:, gs], vn[:, gs], vc[:, gs]], axis=0)
            sink = _per_head_rows([sk_ref[4 * g + j] for j in range(4)])
            s = jnp.where(valid, _dot(_stack_heads(q[:, gs]), kcat, NT), -1e30)
            m = jnp.maximum(jnp.max(s, axis=-1, keepdims=True), sink)
            e = jnp.exp(s - m)
            den = jnp.sum(e, axis=-1, keepdims=True) + jnp.exp(sink - m)
            ys.append(_unstack_heads(_bdot(e / den, vcat)))
            lses.append(_unstack_heads(jnp.broadcast_to(m + jnp.log(den), (4 * BLK, 256))))
        y_ref[...] = jnp.concatenate(ys, axis=1).astype(BF16)
        lse_ref[...] = jnp.concatenate(lses, axis=1)

    blk, ctx = _attn_specs(nb)
    out = pl.BlockSpec((BLK, ATW), lambda i: (i, 0))
    return _pcall(body, name="attn_fwd", grid=(nb,),
                  in_specs=[pl.BlockSpec(memory_space=pltpu.SMEM), blk(0),
                            blk(-1), blk(0), blk(1), ctx, blk(-1), blk(0), blk(1), ctx],
                  out_specs=[out, out],
                  out_shape=[jax.ShapeDtypeStruct((s_len, ATW), BF16),
                             jax.ShapeDtypeStruct((s_len, ATW), F32)], carry=carry)(
                      sinks, qr, k4, k4, k4, k4, v4, v4, v4, v4)


def _attn_bwd(qr, k4, v4, sinks, y, lse, dy, carry=None):
    tt = qr.shape[0]
    s_len = tt - L
    nb = s_len // BLK

    def body(sk_ref, q_ref, kp, ko, kn, kc, vp, vo, vn, vc, y_ref, lse_ref, dy_ref,
             dq_ref, dkw_ref, dvw_ref, dkc_ref, dvc_ref, dsk_ref):
        i = pl.program_id(0)

        @pl.when(i == 0)
        def _():
            dkc_ref[...] = jnp.zeros_like(dkc_ref)
            dvc_ref[...] = jnp.zeros_like(dvc_ref)
            dsk_ref[...] = jnp.zeros_like(dsk_ref)

        valid = _attn_masks(i, nb)
        q = q_ref[...]
        dy_ = dy_ref[...]
        dly = dy_ * y_ref[...].astype(F32)
        lse_ = lse_ref[...]
        dqs = []
        for g in range(2):
            gs = slice(256 * g, 256 * g + 256)
            kcat = jnp.concatenate([kp[:, gs], ko[:, gs], kn[:, gs], kc[:, gs]], axis=0)
            vcat = jnp.concatenate([vp[:, gs], vo[:, gs], vn[:, gs], vc[:, gs]], axis=0)
            q4 = _stack_heads(q[:, gs])
            dy4 = _stack_heads(dy_[:, gs]).astype(BF16)
            lse4 = jnp.max(_stack_heads(lse_[:, gs], fill=-1e30), axis=-1, keepdims=True)
            delta = jnp.sum(_stack_heads(dly[:, gs]), axis=-1, keepdims=True)
            sink = _per_head_rows([sk_ref[4 * g + j] for j in range(4)])
            pr = jnp.where(valid, jnp.exp(_dot(q4, kcat, NT) - lse4), 0.0)
            dsb = (pr * (_dot(dy4, vcat, NT) - delta)).astype(BF16)
            dsink = jnp.exp(sink - lse4) * delta
            for j in range(4):
                dsk_ref[4 * g + j:4 * g + j + 1, :] += jnp.broadcast_to(
                    -jnp.sum(dsink[j * BLK:(j + 1) * BLK], axis=0, keepdims=True), (1, 128))
            dqs.append(_unstack_heads(_dot(dsb, kcat)))
            dkg = _dot(dsb, q4, TN)
            dvg = _dot(pr.astype(BF16), dy4, TN)
            dkw_ref[0, :, gs] = dkg[:3 * BLK]
            dvw_ref[0, :, gs] = dvg[:3 * BLK]
            dkc_ref[:, gs] += dkg[3 * BLK:]
            dvc_ref[:, gs] += dvg[3 * BLK:]
        dq_ref[...] = jnp.concatenate(dqs, axis=1)

    blk, ctx = _attn_specs(nb)
    out = pl.BlockSpec((BLK, ATW), lambda i: (i, 0))
    win = pl.BlockSpec((1, 3 * BLK, ATW), lambda i: (i, 0, 0))
    acc = _full((L, ATW))
    return _pcall(body, name="attn_bwd", grid=(nb,),
                  in_specs=[pl.BlockSpec(memory_space=pltpu.SMEM), blk(0),
                            blk(-1), blk(0), blk(1), ctx, blk(-1), blk(0), blk(1), ctx, out, out, out],
                  out_specs=[out, win, win, acc, acc, _full((8, 128))],
                  out_shape=[jax.ShapeDtypeStruct((s_len, ATW), F32),
                             jax.ShapeDtypeStruct((nb, 3 * BLK, ATW), F32),
                             jax.ShapeDtypeStruct((nb, 3 * BLK, ATW), F32),
                             jax.ShapeDtypeStruct((L, ATW), F32), jax.ShapeDtypeStruct((L, ATW), F32),
                             jax.ShapeDtypeStruct((8, 128), F32)], carry=carry)(
                      sinks, qr, k4, k4, k4, k4, v4, v4, v4, v4, y, lse, dy)


def _attn_post(p, cos, sin, qnw8, knw2, bd512, bd128, dupt, dq, dkw, dvw, dkc, dvc, dp, carry=None):
    tt = p.shape[0]
    s_len = tt - L
    nb = s_len // BLK
    nctx = L // BLK

    def body(q_ref, kv_ref, cos_ref, sin_ref, qw_ref, kw_ref, b5_ref, b1_ref, dupt_ref,
             dq_ref, kwp, kwo, kwn, vwp, vwo, vwn, dkc_ref, dvc_ref, _dp_in,
             dp_ref, dqw_ref, dkw_ref):
        t = pl.program_id(0)
        j = t - nctx

        @pl.when(t == 0)
        def _():
            dqw_ref[...] = jnp.zeros_like(dqw_ref)
            dkw_ref[...] = jnp.zeros_like(dkw_ref)

        is_lat = t >= nctx
        cos_, sin_ = cos_ref[...], sin_ref[...]
        has_p = is_lat & (j >= 1)
        has_n = is_lat & (j <= nb - 2)
        dk4 = (jnp.where(is_lat, kwo[0], dkc_ref[...]) + jnp.where(has_p, kwp[0], 0.0)
               + jnp.where(has_n, kwn[0], 0.0))
        dv4 = (jnp.where(is_lat, vwo[0], dvc_ref[...]) + jnp.where(has_p, vwp[0], 0.0)
               + jnp.where(has_n, vwn[0], 0.0))
        dkr = _dot(dk4, dupt_ref[...], prec=HI)
        dv = _dot(dv4, dupt_ref[...], prec=HI)
        kv = kv_ref[...]
        k = kv[:, :128]
        kw = kw_ref[...]
        rk = lax.rsqrt(_head_mean(k * k, b1_ref[...]) + EPS)
        xk = k * rk
        dkn = dkr * cos_ + _rot(dkr * sin_)
        dxk = dkn * kw
        dk = rk * (dxk - xk * _head_mean(dxk * xk, b1_ref[...]))
        dkw_ref[...] += jnp.sum(dkn * xk, axis=0, keepdims=True)
        q = q_ref[...]
        qw = qw_ref[...]
        rq = lax.rsqrt(_head_mean(q * q, b5_ref[...]) + EPS)
        xq = q * rq
        cos4 = jnp.concatenate([cos_] * 4, axis=1)
        sin4 = jnp.concatenate([sin_] * 4, axis=1)
        dqr = jnp.where(is_lat, dq_ref[...], 0.0) * (HDIM ** -0.5)
        dqn = dqr * cos4 + _rot(dqr * sin4)
        dxq = dqn * qw
        dqraw = rq * (dxq - xq * _head_mean(dxq * xq, b5_ref[...]))
        dqw_ref[...] += jnp.sum(dqn * xq, axis=0, keepdims=True)
        dp_ref[...] = jnp.concatenate([dqraw, dk, dv], axis=1).astype(BF16)

    row = lambda w, cb: pl.BlockSpec((BLK, w), lambda t: (t, cb))
    lat = pl.BlockSpec((BLK, ATW), lambda t: (jnp.maximum(t - nctx, 0), 0))

    def part(off):
        return pl.BlockSpec((1, BLK, ATW), lambda t: (jnp.clip(t - nctx + off, 0, nb - 1), 1 - off, 0))

    cacc = pl.BlockSpec((BLK, ATW), lambda t: (jnp.minimum(t, nctx - 1), 0))
    return _pcall(body, name="attn_post", grid=(tt // BLK,),
                  in_specs=[row(ATW, C_QRAW), row(256, C_KV), row(128, 0), row(128, 0),
                            _full((1, ATW)), _full((1, 128)), _full((ATW, ATW)), _full((128, 128)),
                            _full((ATW, 128)), lat, part(-1), part(0), part(1), part(-1), part(0), part(1),
                            cacc, cacc, ANY],
                  out_specs=[pl.BlockSpec((BLK, 768), lambda t: (t, C_QKV)), _full((1, ATW)), _full((1, 128))],
                  out_shape=[jax.ShapeDtypeStruct(dp.shape, BF16), jax.ShapeDtypeStruct((1, ATW), F32),
                             jax.ShapeDtypeStruct((1, 128), F32)],
                  aliases={18: 0}, carry=carry)(p, p, cos, sin, qnw8, knw2, bd512, bd128, dupt,
                                   dq, dkw, dkw, dkw, dvw, dvw, dvw, dkc, dvc, dp)


def _merge(ah, aa, p):
    s_len = ah.shape[0]

    def body(ah_ref, aa_ref, gh_ref, ga_ref, m_ref):
        m_ref[...] = (_sig(gh_ref[...]) * ah_ref[...] + _sig(ga_ref[...]) * aa_ref[...]).astype(BF16)

    row = pl.BlockSpec((TM, D), lambda i: (i, 0))
    return _pcall(body, name="merge", grid=(s_len // TM,),
                  in_specs=[row, row, pl.BlockSpec((TM, D), lambda i: (i + 1, 2)),
                            pl.BlockSpec((TM, D), lambda i: (i + 1, 3))],
                  out_specs=row, out_shape=jax.ShapeDtypeStruct((s_len, D), BF16))(ah, aa, p, p)


def _merge_bwd(dm, ah, aa, p, carry=None):
    tt = p.shape[0]
    s_len = tt - L

    def body(dm_ref, ah_ref, aa_ref, gh_ref, ga_ref, dp_ref, dmh_ref, dma_ref):
        i = pl.program_id(0)

        @pl.when(i == 0)
        def _():
            dp_ref[...] = jnp.zeros_like(dp_ref)

        @pl.when(i >= 1)
        def _():
            dm_ = dm_ref[...]
            sh, sa = _sig(gh_ref[...]), _sig(ga_ref[...])
            dp_ref[...] = jnp.concatenate([dm_ * ah_ref[...] * sh * (1.0 - sh),
                                           dm_ * aa_ref[...] * sa * (1.0 - sa)], axis=1).astype(BF16)
            dmh_ref[...] = (dm_ * sh).astype(BF16)
            dma_ref[...] = (dm_ * sa).astype(BF16)

    lat = pl.BlockSpec((TM, D), lambda i: (jnp.maximum(i - 1, 0), 0))
    return _pcall(body, name="merge_bwd", grid=(tt // TM,),
                  in_specs=[lat, lat, lat, pl.BlockSpec((TM, D), lambda i: (i, 2)),
                            pl.BlockSpec((TM, D), lambda i: (i, 3))],
                  out_specs=[pl.BlockSpec((TM, 2 * D), lambda i: (i, C_GATES)), lat, lat],
                  out_shape=[jax.ShapeDtypeStruct((tt, NCOL), BF16), jax.ShapeDtypeStruct((s_len, D), BF16),
                             jax.ShapeDtypeStruct((s_len, D), BF16)], carry=carry)(dm, ah, aa, p, p)


def _local_step(x, ctx, tgt, mod, modc, nw1, nw2, lg, hw, qnw, knw, sinks,
                w_in, wts, dist=None):
    s_len = x.shape[0]
    tt = s_len + L
    tok = jnp.concatenate([ctx, x], axis=0)
    ss1 = jnp.stack([modc, mod[0:2]])
    ss2 = mod[3:5][None]
    g1, g2 = mod[2:3], mod[5:6]
    hw4 = jnp.tile(hw, (1, 4))
    qnw8 = jnp.tile(qnw, (1, 8))
    knw2 = jnp.tile(knw, (1, 2))
    cos, sin = _rope_tables(s_len)
    bd512, bd128 = _blockdiag(ATW, HDIM), _blockdiag(128, HDIM)
    dupm = _dup_matrix()
    dup, dupt = jnp.asarray(dupm, BF16), jnp.asarray(dupm.T, F32)
    tmt = tt

    def four(b):
        return b.reshape(4, 2 * b.shape[1], b.shape[2])

    def halves(g):
        return g.reshape(4, 2, g.shape[1] // 2, g.shape[2])

    h = _modulate(tok, nw1, ss1, name="mod1", sel=lambda i: jnp.minimum(i, 1))
    if dist is None:
        bh4, ba4, w_o, g4, u4, dn4 = wts
        p = _mm_in(h, w_in, tmt)
        o0, st0 = _hgrn_fwd(p, lg, rev=False)
        o1, st1 = _hgrn_fwd(p, lg, rev=True)
    else:
        core, chip = dist
        half = wts[3].shape[1] // 2
        p, first = _mm_in(h, w_in, tmt, carry=_carry_join(_carry_gather(list(wts[0:3])),
                                                          _carry_gather([wts[3]], rows=[(0, half)])))
        (o0, st0), (g8,) = _hgrn_fwd(p, lg, rev=False, carry=_carry_gather([first[3]], rows=[(half, half)]))
        o1, st1 = _hgrn_fwd(p, lg, rev=True)
        bh4, ba4, w_o, g4 = four(first[0]), four(first[1]), four(first[2]).reshape(D, D), four(g8)
    y_hg = _readout(o0, o1, p, hw4)
    qr, k4, v4 = _qk_prep(p, cos, sin, qnw8, knw2, bd512, bd128, dup)
    if dist is None:
        y_at, lse = _attn_fwd(qr, k4, v4, sinks)
    else:
        (y_at, lse), (u8,) = _attn_fwd(qr, k4, v4, sinks, carry=_carry_gather([wts[4]]))
        u4 = four(u8)
    ah = _mm_cs(y_hg, bh4, name="mm_bh")
    aa = _mm_cs(y_at, ba4, name="mm_ba")
    mixed = _merge(ah, aa, p)
    ao = _mm(mixed, w_o, name="mm_o", tm=512, tn=D, tk=D)
    x1, h2 = _res1_mod2(x, ao, g1, nw2, ss2)
    if dist is None:
        a4, b4, z4 = _ffn_up(h2, g4, u4)
    else:
        (a4, b4, z4), (dn8,) = _ffn_up(h2, g4, u4, carry=_carry_gather([wts[5]]))
        dn4 = four(dn8)
    y = _ffn_down(z4, dn4)
    sq, dx2, dyb, dg2 = _loss_head(x1, y, g2, tgt)

    da4, db4 = _ffn_dz(dyb, dn4, a4, b4)
    g_dn = _ffn_gdn(z4, dyb)
    if dist is None:
        dh2 = _ffn_dh2(da4, db4, g4, u4)
    else:
        dn_units = [halves(g_dn)]
        dh2, dn_recv = _ffn_dh2(da4, db4, g4, u4, carry=_carry_pairx(dn_units))
        dn_pairs = _rs_pair_add(dn_units, dn_recv, core)
    if dist is None:
        g_g, g_u = _ffn_ggu(h2, da4, db4)
    else:
        (g_g, g_u), c_dn = _ffn_ggu(h2, da4, db4, carry=_carry_chipx(dn_pairs))
        red_dn = _rs_chip_add(dn_pairs, c_dn, core, chip)
    dx1, dattn, dss2, dnw2, dg1 = _mod2_bwd(x1, dh2, dx2, ao, nw2, ss2, g1)
    dm = _mm(dattn, w_o, name="mm_dm", mode="nt", tm=512, tn=D, tk=D)
    g_o = _mm(mixed, dattn, name="mm_go", mode="tn", tm=D, tn=D, tk=512)
    if dist is None:
        dp, dmh, dma = _merge_bwd(dm, ah, aa, p)
    else:
        gu_units = [halves(g_g), halves(g_u)]
        (dp, dmh, dma), gu_recv = _merge_bwd(dm, ah, aa, p, carry=_carry_pairx(gu_units))
        ffn_pairs = list(dn_pairs) + list(_rs_pair_add(gu_units, gu_recv, core))
    dy_hg = _mm_cs_nt(dmh, bh4, name="mm_dyh")
    dy_at = _mm_cs_nt(dma, ba4, name="mm_dya")
    g_bh = _mm_cs_tn(y_hg, dmh, D // 4, name="mm_gbh")
    g_ba = _mm_cs_tn(y_at, dma, D // 4, name="mm_gba")
    if dist is None:
        dp, do, dhw4 = _readout_bwd(o0, o1, p, hw4, dy_hg, dp)
        dq, dkw, dvw, dkc, dvc, dsk = _attn_bwd(qr, k4, v4, sinks, y_at, lse, dy_at)
        dp, dqnw8, dknw2 = _attn_post(p, cos, sin, qnw8, knw2, bd512, bd128, dupt, dq, dkw, dvw, dkc, dvc, dp)
    else:
        mix_units = [halves(g_bh), halves(g_ba), halves(g_o.reshape(4, D // 4, D))]
        (dp, do, dhw4), mix_recv = _readout_bwd(o0, o1, p, hw4, dy_hg, dp, carry=_carry_pairx(mix_units))
        mix_pairs = _rs_pair_add(mix_units, mix_recv, core)
        (dq, dkw, dvw, dkc, dvc, dsk), bwd = _attn_bwd(
            qr, k4, v4, sinks, y_at, lse, dy_at,
            carry=_carry_join(_carry_chipx(ffn_pairs[1:2]), _carry_sibx(red_dn)))
        red_g = _rs_chip_add(ffn_pairs[1:2], bwd[0:1], core, chip)
        (dp, dqnw8, dknw2), post = _attn_post(
            p, cos, sin, qnw8, knw2, bd512, bd128, dupt, dq, dkw, dvw, dkc, dvc, dp,
            carry=_carry_join(_carry_chipx(ffn_pairs[2:3]), _carry_sibx(red_g)))
        red_u = _rs_chip_add(ffn_pairs[2:3], post[0:1], core, chip)
    if dist is None:
        dp, dv0, dq0, dlg0 = _hgrn_bwd(p, lg, do, st0, dp, None, rev=False)
        dp, dlg1 = _hgrn_bwd(p, lg, do, st1, dp, (dv0, dq0), rev=True)
    else:
        (dp, dv0, dq0, dlg0), mid = _hgrn_bwd(p, lg, do, st0, dp, None, rev=False,
                                              carry=_carry_join(_carry_chipx(mix_pairs), _carry_sibx(red_u)))
        mix_reds = _rs_chip_add(mix_pairs, mid[0:3], core, chip)
        (dp, dlg1), mix_done = _hgrn_bwd(p, lg, do, st1, dp, (dv0, dq0), rev=True, carry=_carry_sibx(mix_reds))
        ffn_done = bwd[1:2] + post[1:2] + mid[3:4]
    dh = _mm_dh(dp, w_in, tmt)
    g_in = _mm_gin(dp, h, tmt)
    if dist is None:
        gx, dss1, dnw1 = _mod1_bwd(tok, dh, dx1, nw1, ss1)
        rs = None
    else:
        in_units = [halves(g_in.reshape(4, NCOL // 4, D))]
        (gx, dss1, dnw1), in_recv = _mod1_bwd(tok, dh, dx1, nw1, ss1, carry=_carry_pairx(in_units))
        rs = dict(ffn_done=ffn_done, mix_done=mix_done, in_units=in_units, in_recv=in_recv)

    dmod = jnp.concatenate([dss1[1], dg1, dss2, dg2], axis=0)
    dmodc = dss1[0]
    raw = (dss1, dg1, dss2, dg2, dnw1, dnw2, dhw4, dqnw8, dknw2, dsk, dlg0, dlg1)
    small = dict(raw=raw, dmod=dmod, dmodc=dmodc, dnw1=dnw1, dnw2=dnw2,
                 dhw=dhw4.reshape(4, HGD).sum(0, keepdims=True),
                 dqnw=dqnw8.reshape(8, HDIM).sum(0, keepdims=True),
                 dknw=dknw2.reshape(2, HDIM).sum(0, keepdims=True),
                 dsinks=dsk[:, 0], dlg=jnp.concatenate([dlg0, dlg1], axis=0))
    big = dict(w_in=g_in, w_bh=g_bh, w_ba=g_ba, w_o=g_o, w_g=g_g, w_u=g_u, w_dn=g_dn)
    return sq, gx, big, small, rs


def _place():
    x, y, c = lax.axis_index("x"), lax.axis_index("y"), lax.axis_index("c")
    return x, y, c


def _gather_blocks(x_refs, out_refs, send_sems, recv_sems, local_sems):
    n = len(out_refs)
    x, y, c = _place()
    me, sibling = (x, y, c), (x, y, 1 - c)
    chips = [(1 - x, y), (x, 1 - y), (1 - x, 1 - y)]

    def slot(u, px, py, pc):
        return out_refs[u].at[4 * px + 2 * py + pc]

    def copy(u, k, block, to, src=None):
        return pltpu.make_async_remote_copy(
            src_ref=slot(u, *block) if src is None else src, dst_ref=slot(u, *block),
            send_sem=send_sems.at[u, k], recv_sem=recv_sems.at[u, k], device_id=to, device_id_type=MESH)

    mines = []
    if x_refs is not None:
        mines = [pltpu.make_async_copy(x_refs[u], slot(u, *me), local_sems.at[u]) for u in range(n)]
    for cp in mines:
        cp.start()
    first = []
    for u in range(n):
        src = None if x_refs is None else x_refs[u]
        first.append(copy(u, 0, me, sibling, src=src))
        first += [copy(u, 1 + j, me, (*chip, c), src=src) for j, chip in enumerate(chips)]
    for cp in first:
        cp.start()
    passed = []
    for j, chip in enumerate(chips):
        for u in range(n):
            copy(u, 1 + j, (*chip, c), me).wait_recv()
            fwd = copy(u, 4 + j, (*chip, c), sibling)
            fwd.start()
            passed.append(fwd)
    for u in range(n):
        copy(u, 0, sibling, me).wait_recv()
    for j, chip in enumerate(chips):
        for u in range(n):
            copy(u, 4 + j, (*chip, 1 - c), me).wait_recv()
    for cp in first + passed:
        cp.wait_send()
    for cp in mines:
        cp.wait()


def _gather_sems(n):
    return [pltpu.SemaphoreType.DMA((n, 7)), pltpu.SemaphoreType.DMA((n, 7)), pltpu.SemaphoreType.DMA((n,))]


def _allgather(blks, *, name, in_vmem):
    n = len(blks)
    space = pltpu.VMEM if in_vmem else pl.ANY

    def body(*refs):
        _gather_blocks(refs[:n], refs[n:2 * n], *refs[2 * n:])

    return pl.pallas_call(
        body, name=name, out_shape=[jax.ShapeDtypeStruct((8,) + b.shape, b.dtype) for b in blks],
        in_specs=[pl.BlockSpec(memory_space=space)] * n, out_specs=[pl.BlockSpec(memory_space=space)] * n,
        scratch_shapes=_gather_sems(n))(*blks)


def _cast_place(ws, c, dev):
    n = len(ws)

    def body(s_ref, *refs):
        for u in range(n):
            refs[n + u][0] = refs[u][...].astype(BF16)

    in_specs, out_specs, out_shape = [], [], []
    for w in ws:
        q, cols = w.shape[0] // 4, w.shape[1]
        in_specs.append(pl.BlockSpec((q, cols), lambda i, s: (2 * s[0] + i, 0)))
        out_specs.append(pl.BlockSpec((1, q, cols), lambda i, s: (s[1], i, 0)))
        out_shape.append(jax.ShapeDtypeStruct((8, 2 * q, cols), BF16))
    return pl.pallas_call(
        body, name="cast_place",
        grid_spec=pltpu.PrefetchScalarGridSpec(num_scalar_prefetch=1, grid=(2,), in_specs=in_specs,
                                               out_specs=out_specs),
        out_shape=_out_hbm(out_shape),
        compiler_params=pltpu.CompilerParams(vmem_limit_bytes=48 << 20))(jnp.stack([c, dev]), *_in_hbm(ws))


def _gather_phases(out_refs, send_sems, recv_sems, rows=None):
    n = len(out_refs)
    x, y, c = _place()
    me, sibling = (x, y, c), (x, y, 1 - c)
    chips = [(1 - x, y), (x, 1 - y), (1 - x, 1 - y)]

    def copy(u, k, block, to):
        px, py, pc = block
        ref = out_refs[u].at[4 * px + 2 * py + pc]
        if rows is not None and rows[u] is not None:
            ref = ref.at[pl.ds(rows[u][0], rows[u][1])]
        return pltpu.make_async_remote_copy(src_ref=ref, dst_ref=ref, send_sem=send_sems.at[u, k],
                                            recv_sem=recv_sems.at[u, k], device_id=to, device_id_type=MESH)

    def start():
        for u in range(n):
            copy(u, 0, me, sibling).start()
            for j, chip in enumerate(chips):
                copy(u, 1 + j, me, (*chip, c)).start()

    def mid():
        for j, chip in enumerate(chips):
            for u in range(n):
                copy(u, 1 + j, (*chip, c), me).wait_recv()
                copy(u, 4 + j, (*chip, c), sibling).start()

    def end():
        for u in range(n):
            copy(u, 0, sibling, me).wait_recv()
        for j, chip in enumerate(chips):
            for u in range(n):
                copy(u, 4 + j, (*chip, 1 - c), me).wait_recv()
        for u in range(n):
            copy(u, 0, me, sibling).wait_send()
            for j, chip in enumerate(chips):
                copy(u, 1 + j, me, (*chip, c)).wait_send()
                copy(u, 4 + j, (*chip, c), sibling).wait_send()

    return start, mid, end


def _carry_gather(bufs, rows=None):
    n = len(bufs)
    return _Carry(bufs, [jax.ShapeDtypeStruct(b.shape, b.dtype) for b in bufs], {u: u for u in range(n)},
                  [pltpu.SemaphoreType.DMA((n, 7)), pltpu.SemaphoreType.DMA((n, 7))],
                  lambda ins, outs, sems: _gather_phases(outs, *sems, rows=rows))


def _allgather_inplace(bufs, *, name):
    n = len(bufs)

    def body(*refs):
        for phase in _gather_phases(refs[n:2 * n], *refs[2 * n:]):
            phase()

    return pl.pallas_call(
        body, name=name, out_shape=[jax.ShapeDtypeStruct(b.shape, b.dtype) for b in bufs],
        in_specs=[ANY] * n, out_specs=[ANY] * n, input_output_aliases={u: u for u in range(n)},
        scratch_shapes=[pltpu.SemaphoreType.DMA((n, 7)), pltpu.SemaphoreType.DMA((n, 7))])(*bufs)


def _ag_small(raw):
    def body(dss1, dg1, dss2, dg2, dnw1, dnw2, dhw4, dqnw8, dknw2, dsk, dlg0, dlg1,
             out_ref, tot_ref, blk, send_sems, recv_sems, local_sems):
        blk[...] = jnp.zeros_like(blk)
        blk[0:2, :] = dss1[1]
        blk[2:3, :] = dg1[...]
        blk[3:5, :] = dss2[...]
        blk[5:6, :] = dg2[...]
        blk[6:8, :] = dss1[0]
        blk[8:9, :] = dnw1[...]
        blk[9:10, :] = dnw2[...]
        blk[10:11, 0:HGW] = dhw4[...]
        blk[10:11, HGW:D] = dqnw8[...]
        blk[11:12, 0:128] = dknw2[...]
        blk[12:14, 0:HGW] = dlg0[0]
        blk[14:16, 0:HGW] = dlg1[0]
        blk[16:24, 0:128] = dsk[...]
        _gather_blocks([blk], [out_ref], send_sems, recv_sems, local_sems)
        acc = out_ref[0]
        for i in range(1, 8):
            acc = acc + out_ref[i]
        tot_ref[...] = acc

    vm = pl.BlockSpec(memory_space=pltpu.VMEM)
    return pl.pallas_call(
        body, name="ag_small",
        out_shape=[jax.ShapeDtypeStruct((8, 24, D), F32), jax.ShapeDtypeStruct((24, D), F32)],
        in_specs=[vm] * 12, out_specs=[vm, vm],
        scratch_shapes=[pltpu.VMEM((24, D), F32)] + _gather_sems(1))(*raw)


def _rs_pair_exchange(units):
    n = len(units)

    def body(*refs):
        start, _, end = _pairx_phases(refs[:n], refs[n:2 * n], *refs[2 * n:])
        start()
        end()

    return pl.pallas_call(
        body, name="rs_pair_exchange", out_shape=_pairx_shapes(units),
        in_specs=[ANY] * n, out_specs=[ANY] * n,
        scratch_shapes=[pltpu.SemaphoreType.DMA((n, 4)), pltpu.SemaphoreType.DMA((n, 4))])(*units)


def _pairx_shapes(units):
    return [jax.ShapeDtypeStruct((4,) + g.shape[2:], g.dtype) for g in units]


def _pairx_phases(g_refs, r_refs, send_sems, recv_sems):
    n = len(g_refs)
    x, y, c = _place()
    cps = [pltpu.make_async_remote_copy(
        src_ref=g_refs[u].at[j, 1 - c], dst_ref=r_refs[u].at[j], send_sem=send_sems.at[u, j],
        recv_sem=recv_sems.at[u, j], device_id=(x, y, 1 - c), device_id_type=MESH)
        for u in range(n) for j in range(4)]

    def start():
        for cp in cps:
            cp.start()

    def end():
        for cp in cps:
            cp.wait()

    return start, None, end


def _carry_pairx(units):
    n = len(units)
    return _Carry(units, _pairx_shapes(units), {},
                  [pltpu.SemaphoreType.DMA((n, 4)), pltpu.SemaphoreType.DMA((n, 4))],
                  lambda ins, outs, sems: _pairx_phases(ins, outs, *sems))


def _rs_pair_add(units, recvs, c):
    n = len(units)

    def body(c_ref, *refs):
        for u in range(n):
            refs[2 * n + u][...] = (refs[u][0] + refs[n + u][...]).astype(BF16)

    in_specs, out_specs, out_shape = [], [], []
    for g in units:
        h, w = g.shape[2] // 2, g.shape[3]
        in_specs.append(pl.BlockSpec((1, 1, h, w), lambda j, i, cr: (j, cr[0], i, 0)))
    for g in units:
        h, w = g.shape[2] // 2, g.shape[3]
        in_specs.append(pl.BlockSpec((1, h, w), lambda j, i, cr: (j, i, 0)))
        out_specs.append(pl.BlockSpec((1, h, w), lambda j, i, cr: (j, i, 0)))
        out_shape.append(jax.ShapeDtypeStruct((4, 2 * h, w), BF16))
    return pl.pallas_call(
        body, name="rs_pair_add",
        grid_spec=pltpu.PrefetchScalarGridSpec(num_scalar_prefetch=1, grid=(4, 2), in_specs=in_specs,
                                               out_specs=out_specs),
        out_shape=_out_hbm(out_shape),
        compiler_params=pltpu.CompilerParams(vmem_limit_bytes=48 << 20))(
            c.reshape(1), *_in_hbm(list(units) + list(recvs)))


def _rs_chip_exchange(pairs):
    n = len(pairs)

    def body(*refs):
        start, _, end = _chipx_phases(refs[:n], refs[n:2 * n], *refs[2 * n:])
        start()
        end()

    return pl.pallas_call(
        body, name="rs_chip_exchange", out_shape=[jax.ShapeDtypeStruct(p.shape, p.dtype) for p in pairs],
        in_specs=[ANY] * n, out_specs=[ANY] * n,
        scratch_shapes=[pltpu.SemaphoreType.DMA((n, 3)), pltpu.SemaphoreType.DMA((n, 3))])(*pairs)


def _chipx_phases(p_refs, r_refs, send_sems, recv_sems):
    n = len(p_refs)
    x, y, c = _place()
    k = 2 * x + y
    sends = []
    for d in range(1, 4):
        j = (k + d) % 4
        for u in range(n):
            sends.append(pltpu.make_async_remote_copy(
                src_ref=p_refs[u].at[j], dst_ref=r_refs[u].at[k], send_sem=send_sems.at[u, d - 1],
                recv_sem=recv_sems.at[u, d - 1], device_id=(j // 2, j % 2, c), device_id_type=MESH))

    def start():
        for cp in sends:
            cp.start()

    def end():
        for d in range(1, 4):
            src = (k + 4 - d) % 4
            for u in range(n):
                pltpu.make_async_remote_copy(
                    src_ref=p_refs[u].at[src], dst_ref=r_refs[u].at[src], send_sem=send_sems.at[u, d - 1],
                    recv_sem=recv_sems.at[u, d - 1], device_id=(x, y, c), device_id_type=MESH).wait_recv()
        for cp in sends:
            cp.wait_send()

    return start, None, end


def _carry_chipx(pairs):
    n = len(pairs)
    return _Carry(pairs, [jax.ShapeDtypeStruct(p.shape, p.dtype) for p in pairs], {},
                  [pltpu.SemaphoreType.DMA((n, 3)), pltpu.SemaphoreType.DMA((n, 3))],
                  lambda ins, outs, sems: _chipx_phases(ins, outs, *sems))


def _rs_chip_add(pairs, contribs, c, chip):
    n = len(pairs)

    def body(s_ref, *refs):
        for u in range(n):
            a, b, c_, d = refs[4 * u:4 * u + 4]
            refs[4 * n + u][0] = ((a[0].astype(F32) + b[0].astype(F32)) + c_[0].astype(F32)) + d[0].astype(F32)

    in_specs, out_specs, out_shape, args = [], [], [], []
    for p, r in zip(pairs, contribs):
        h, w = p.shape[1] // 2, p.shape[2]
        in_specs += [pl.BlockSpec((1, h, w), functools.partial(lambda d, i, s: ((s[1] + d) % 4, i, 0), d))
                     for d in range(4)]
        args += [p, r, r, r]
        out_specs.append(pl.BlockSpec((1, h, w), lambda i, s: (s[0], i, 0)))
        out_shape.append(jax.ShapeDtypeStruct((2, 2 * h, w), F32))
    return pl.pallas_call(
        body, name="rs_chip_add",
        grid_spec=pltpu.PrefetchScalarGridSpec(num_scalar_prefetch=1, grid=(2,), in_specs=in_specs,
                                               out_specs=out_specs),
        out_shape=_out_hbm(out_shape),
        compiler_params=pltpu.CompilerParams(vmem_limit_bytes=48 << 20))(jnp.stack([c, chip]), *_in_hbm(args))


def _rs_sibling_gather(reds):
    n = len(reds)

    def body(*refs):
        start, _, end = _sibx_phases(refs[n:2 * n], *refs[2 * n:])
        start()
        end()

    return pl.pallas_call(
        body, name="rs_sibling_gather", out_shape=[jax.ShapeDtypeStruct(r.shape, r.dtype) for r in reds],
        in_specs=[ANY] * n, out_specs=[ANY] * n, input_output_aliases={u: u for u in range(n)},
        scratch_shapes=[pltpu.SemaphoreType.DMA((n,))] * 2)(*reds)


def _sibx_phases(o_refs, send_sems, recv_sems):
    n = len(o_refs)
    x, y, c = _place()
    cps = [pltpu.make_async_remote_copy(
        src_ref=o_refs[u].at[c], dst_ref=o_refs[u].at[c], send_sem=send_sems.at[u], recv_sem=recv_sems.at[u],
        device_id=(x, y, 1 - c), device_id_type=MESH) for u in range(n)]

    def start():
        for cp in cps:
            cp.start()

    def end():
        for u in range(n):
            cps[u].wait_send()
            pltpu.make_async_remote_copy(
                src_ref=o_refs[u].at[1 - c], dst_ref=o_refs[u].at[1 - c], send_sem=send_sems.at[u],
                recv_sem=recv_sems.at[u], device_id=(x, y, 1 - c), device_id_type=MESH).wait_recv()

    return start, None, end


def _carry_sibx(reds):
    n = len(reds)
    return _Carry(reds, [jax.ShapeDtypeStruct(r.shape, r.dtype) for r in reds], {u: u for u in range(n)},
                  [pltpu.SemaphoreType.DMA((n,))] * 2, lambda ins, outs, sems: _sibx_phases(outs, *sems))


def _prologue(blk, c_ctx, w, b, in8):
    n = w.shape[1]

    def body(blk_ref, cctx_ref, w_ref, b_ref, _in_in, g0_ref, c16_ref, g1_ref, in_ref, mod_s,
             s1, r1, l1, s2, r2, l2, s3, r3):
        start, mid, end = _gather_phases([in_ref], s3, r3)
        start()
        _gather_blocks([blk_ref], [g0_ref], s1, r1, l1)
        c16 = jnp.concatenate([g0_ref[i, 0:1, :] for i in range(8)] + [cctx_ref[...], jnp.zeros((7, D), F32)],
                              axis=0)
        c16_ref[...] = c16
        mod_s[...] = _dot(c16 * _sig(c16), w_ref[...], prec=HI) + b_ref[...]
        _gather_blocks([mod_s], [g1_ref], s2, r2, l2)
        mid()
        end()

    vm = pl.BlockSpec(memory_space=pltpu.VMEM)
    return pl.pallas_call(
        body, name="prologue",
        out_shape=[jax.ShapeDtypeStruct((8, 8, D), F32), jax.ShapeDtypeStruct((16, D), F32),
                   jax.ShapeDtypeStruct((8, 16, n), F32), jax.ShapeDtypeStruct(in8.shape, in8.dtype)],
        in_specs=[vm, vm, vm, vm, ANY], out_specs=[vm, vm, vm, ANY], input_output_aliases={4: 3},
        scratch_shapes=[pltpu.VMEM((16, n), F32)] + _gather_sems(1) + _gather_sems(1)
        + [pltpu.SemaphoreType.DMA((1, 7)), pltpu.SemaphoreType.DMA((1, 7))],
        compiler_params=pltpu.CompilerParams(vmem_limit_bytes=48 << 20))(blk, c_ctx, w, b, in8)


def _ada_bwd(c16, dmod16, w):
    n = w.shape[1]
    tn = 512

    def body(c_ref, d_ref, w_ref, gw_ref, gc_ref):
        j = pl.program_id(0)

        @pl.when(j == 0)
        def _():
            gc_ref[...] = jnp.zeros_like(gc_ref)

        cc = c_ref[...]
        dm = d_ref[...]
        gw_ref[...] = _dot(cc * _sig(cc), dm, TN, prec=HI)
        gc_ref[...] += _dot(dm, w_ref[...], NT, prec=HI)

    return _pcall(body, name="ada_bwd", grid=(n // tn,),
                  in_specs=[_full((16, D)), pl.BlockSpec((16, tn), lambda j: (0, j)),
                            pl.BlockSpec((D, tn), lambda j: (0, j))],
                  out_specs=[pl.BlockSpec((D, tn), lambda j: (0, j)), _full((16, D))],
                  out_shape=[jax.ShapeDtypeStruct((D, n), F32),
                             jax.ShapeDtypeStruct((16, D), F32)])(c16, dmod16, w)


def _adam_math(w, g, m, v):
    c1 = 1.0 - ADAM_B1 ** ADAM_STEP
    c2 = 1.0 - ADAM_B2 ** ADAM_STEP
    nm = ADAM_B1 * m + (1.0 - ADAM_B1) * g
    nv = ADAM_B2 * v + (1.0 - ADAM_B2) * (g * g)
    return -ADAM_LR * ((nm / c1) / (jnp.sqrt(nv / c2) + ADAM_EPS) + ADAM_WD * w), nm, nv


def _adamw_small(ws, gs, ms, vs):
    n = len(ws)

    def body(*refs):
        for u in range(n):
            d_, nm, nv = _adam_math(refs[u][...], refs[n + u][...], refs[2 * n + u][...], refs[3 * n + u][...])
            refs[4 * n + u][...] = d_
            refs[5 * n + u][...] = nm
            refs[6 * n + u][...] = nv

    specs = [_full(w.shape) for w in ws]
    shapes = [jax.ShapeDtypeStruct(w.shape, F32) for w in ws]
    out = _pcall(body, name="adamw_small", grid=(1,), in_specs=specs * 4, out_specs=specs * 3,
                 out_shape=shapes * 3)(*ws, *gs, *ms, *vs)
    return out[:n], out[n:2 * n], out[2 * n:]


def _cctx_grad(parts, c_ctx):
    def body(p_ref, c_ref, o_ref):
        acc = p_ref[0:1, :]
        for k in range(1, 4):
            acc = acc + p_ref[k:k + 1, :]
        cc = c_ref[...]
        s = _sig(cc)
        o_ref[...] = acc * (s * (1.0 + cc * (1.0 - s)))

    return _pcall(body, name="cctx_grad", grid=(1,), in_specs=[_full(parts.shape), _full((1, D))],
                  out_specs=_full((1, D)), out_shape=jax.ShapeDtypeStruct((1, D), F32))(parts, c_ctx)


ADAM_STEPS = 8


def _adamw_multi(ws, gs, ms, vs, *, name, carry=None):
    n = len(ws)

    def body(*refs):
        for u in range(n):
            refs[4 * n + u][...], refs[5 * n + u][...], refs[6 * n + u][...] = _adam_math(
                refs[u][...], refs[n + u][...], refs[2 * n + u][...], refs[3 * n + u][...])

    specs = [pl.BlockSpec((w.shape[0] // ADAM_STEPS, w.shape[1]), lambda i: (i, 0)) for w in ws]
    shapes = [jax.ShapeDtypeStruct(w.shape, F32) for w in ws]
    res = _pcall(body, name=name, grid=(ADAM_STEPS,), in_specs=specs * 4, out_specs=specs * 3,
                 out_shape=shapes * 3, carry=carry)(*ws, *gs, *ms, *vs)
    out, extra = res if carry is not None else (res, None)
    return (out[:n], out[n:2 * n], out[2 * n:]), extra


def kernel(x, c, ctx, c_ctx, w_ada, b_ada, norm_mix_w, norm_ffn_w, w_in, hgrn_lb_logits, hgrn_norm_w, q_norm_w, k_norm_w, attn_sinks, w_branch_hgrn, w_branch_attn, w_out, w_ffn_gate, w_ffn_up, w_ffn_down, loss_target, m_c_ctx, m_w_ada, m_b_ada, m_norm_mix_w, m_norm_ffn_w, m_w_in, m_hgrn_lb_logits, m_hgrn_norm_w, m_q_norm_w, m_k_norm_w, m_attn_sinks, m_w_branch_hgrn, m_w_branch_attn, m_w_out, m_w_ffn_gate, m_w_ffn_up, m_w_ffn_down, v_c_ctx, v_w_ada, v_b_ada, v_norm_mix_w, v_norm_ffn_w, v_w_in, v_hgrn_lb_logits, v_hgrn_norm_w, v_q_norm_w, v_k_norm_w, v_attn_sinks, v_w_branch_hgrn, v_w_branch_attn, v_w_out, v_w_ffn_gate, v_w_ffn_up, v_w_ffn_down):
    xi, yi, ci = _place()
    chip = 2 * xi + yi
    dev = 2 * chip + ci
    s_len = x.shape[1]

    shards = [w_in[0].T, w_branch_hgrn[0], w_branch_attn[0], w_out[0], w_ffn_gate[0].T, w_ffn_up[0].T,
              w_ffn_down[0]]
    bufs = _cast_place(shards, ci, dev)

    lbrow = jnp.pad(hgrn_lb_logits.reshape(1, 512), ((0, 0), (0, D - 512)))
    blk = jnp.concatenate([c, lbrow, jnp.zeros((6, D), F32)], axis=0)
    nada = w_ada.shape[2]
    b_sh = lax.dynamic_slice(b_ada, (0, chip * nada), (1, nada))
    g0, c16, g1, in8 = _prologue(blk, c_ctx[None], w_ada[0], b_sh, bufs[0])
    lg = g0[0::2, 1, :512].reshape(4, 2, 2, 128).transpose(1, 2, 0, 3).reshape(2, 2, HGW)
    modall = g1[0::2].transpose(1, 0, 2).reshape(16, 4 * nada)
    mod = lax.dynamic_slice(modall, (dev, 0), (1, 6 * D)).reshape(6, D)
    modc = modall[8].reshape(6, D)[:2]

    sq, gx, _, small, rs = _local_step(
        x[0], ctx[0], loss_target[0], mod, modc, norm_mix_w, norm_ffn_w, lg, hgrn_norm_w, q_norm_w,
        k_norm_w, attn_sinks[0], in8.reshape(NCOL, D), bufs[1:], dist=(ci, chip))
    loss = lax.psum(0.5 * jnp.sum(sq) / D, ("x", "y", "c"))

    def whole(r):
        return r.reshape(2 * r.shape[1], r.shape[2])

    g_dn, g_g, g_u = [whole(r) for r in rs["ffn_done"]]
    g_bh, g_ba, g_o = [whole(r) for r in rs["mix_done"]]
    in_pairs = _rs_pair_add(rs["in_units"], rs["in_recv"], ci)

    g2, tot = _ag_small(small["raw"])
    dmodc_tot = jnp.pad(tot[6:8].reshape(1, 2 * D), ((0, 0), (0, 4 * D)))
    g_b_ada = tot[0:6].reshape(1, 6 * D) + dmodc_tot
    dmod16 = jnp.concatenate([g2[:, 0:6].reshape(8, 6 * D), dmodc_tot, jnp.zeros((7, 6 * D), F32)], axis=0)
    g_w_ada, gc_part = _ada_bwd(c16, lax.dynamic_slice(dmod16, (0, chip * nada), (16, nada)), w_ada[0])
    g3, = _allgather([gc_part[8:16]], name="ag_cctx", in_vmem=True)
    g_c_ctx = _cctx_grad(g3[0::2, 0], c_ctx[None])[0]
    g_nw1 = tot[8:9]
    g_nw2 = tot[9:10]
    g_hw = tot[10, :HGW].reshape(4, HGD).sum(0, keepdims=True)
    g_qnw = tot[10, HGW:].reshape(8, HDIM).sum(0, keepdims=True)
    g_knw = tot[11, :128].reshape(2, HDIM).sum(0, keepdims=True)
    g_sinks = tot[16:24, 0][None]
    g_lg = lax.dynamic_slice(tot[12:16, :HGW].reshape(2, 2, HGW), (0, 0, chip * 128), (2, 2, 128))

    names = ["c_ctx", "w_ada", "b_ada", "norm_mix_w", "norm_ffn_w", "w_in", "hgrn_lb_logits", "hgrn_norm_w",
             "q_norm_w", "k_norm_w", "attn_sinks", "w_branch_hgrn", "w_branch_attn", "w_out", "w_ffn_gate",
             "w_ffn_up", "w_ffn_down"]
    ws = dict(zip(names, [c_ctx, w_ada, b_ada, norm_mix_w, norm_ffn_w, w_in, hgrn_lb_logits, hgrn_norm_w,
                          q_norm_w, k_norm_w, attn_sinks, w_branch_hgrn, w_branch_attn, w_out, w_ffn_gate,
                          w_ffn_up, w_ffn_down]))
    ms = dict(zip(names, [m_c_ctx, m_w_ada, m_b_ada, m_norm_mix_w, m_norm_ffn_w, m_w_in, m_hgrn_lb_logits,
                          m_hgrn_norm_w, m_q_norm_w, m_k_norm_w, m_attn_sinks, m_w_branch_hgrn,
                          m_w_branch_attn, m_w_out, m_w_ffn_gate, m_w_ffn_up, m_w_ffn_down]))
    vs = dict(zip(names, [v_c_ctx, v_w_ada, v_b_ada, v_norm_mix_w, v_norm_ffn_w, v_w_in, v_hgrn_lb_logits,
                          v_hgrn_norm_w, v_q_norm_w, v_k_norm_w, v_attn_sinks, v_w_branch_hgrn,
                          v_w_branch_attn, v_w_out, v_w_ffn_gate, v_w_ffn_up, v_w_ffn_down]))
    transposed = ("w_in", "w_ffn_gate", "w_ffn_up")

    def view(a, n):
        return a[0].T if n in transposed else a[0]

    def unview(a, n):
        return a.T[None] if n in transposed else a[None]

    delta, new_m, new_v, grads = {}, {}, {}, {}

    def big_adamw(group, gs, name, carry=None):
        (d_, m_, v_), extra = _adamw_multi([view(ws[n], n) for n in group], gs, [view(ms[n], n) for n in group],
                                           [view(vs[n], n) for n in group], name=name, carry=carry)
        for i, n in enumerate(group):
            grads[n], delta[n], new_m[n], new_v[n] = (unview(gs[i], n), unview(d_[i], n), unview(m_[i], n),
                                                      unview(v_[i], n))
        return extra

    in_contribs = big_adamw(["w_ffn_down", "w_ffn_gate", "w_ffn_up", "w_out", "w_branch_hgrn", "w_branch_attn"],
                            [g_dn, g_g, g_u, g_o, g_bh, g_ba], "adamw_first", carry=_carry_chipx(in_pairs))
    in_reds = _rs_chip_add(in_pairs, in_contribs, ci, chip)
    g_in, = [whole(r) for r in _rs_sibling_gather(in_reds)]
    big_adamw(["w_in", "w_ada"], [g_in, g_w_ada], "adamw_second")
    grads.update(c_ctx=g_c_ctx, b_ada=g_b_ada, norm_mix_w=g_nw1, norm_ffn_w=g_nw2, hgrn_lb_logits=g_lg,
                 hgrn_norm_w=g_hw, q_norm_w=g_qnw, k_norm_w=g_knw, attn_sinks=g_sinks)
    small_names = [n for n in names if n not in delta]

    def two_d(a):
        return a.reshape(1, -1) if a.ndim == 1 else a

    sd, sm_, sv = _adamw_small(*[[two_d(d[n]) for n in small_names] for d in (ws, grads, ms, vs)])
    for i, n in enumerate(small_names):
        for dst, src in ((delta, sd), (new_m, sm_), (new_v, sv)):
            dst[n] = src[i].reshape(ws[n].shape)
    return (loss, gx[None], *[grads[n] for n in names], *[delta[n] for n in names],
            *[new_m[n] for n in names], *[new_v[n] for n in names])
```

```python
import functools

import numpy as np
import jax
import jax.numpy as jnp
from jax import lax
from jax.experimental import pallas as pl
from jax.experimental.pallas import tpu as pltpu

F32 = jnp.float32
BF16 = jnp.bfloat16
HI = lax.Precision.HIGHEST
MESH = pl.DeviceIdType.MESH

D = 1024
L = 256
TM = 256
HGW = 512
HGD = 128
CH = 32
ATW = 512
HDIM = 64
BLK = 128
GRID_W = 64
DFF = 2816
NCOL = 5376
EPS = 1e-6
ROPE_THETA = 10000.0

C_FB, C_INP, C_QHG, C_FF = 0, 1, 2, 3
C_GATES = 1
C_GHG, C_QRAW = 8, 9
C_KV = 20
C_QKV = 6

ADAM_LR, ADAM_B1, ADAM_B2, ADAM_EPS, ADAM_WD, ADAM_STEP = 0.001, 0.9, 0.999, 1e-08, 0.01, 10

NN = (((1,), (0,)), ((), ()))
NT = (((1,), (1,)), ((), ()))
TN = (((0,), (0,)), ((), ()))


def _dot(a, b, dims=NN, prec=None):
    return lax.dot_general(a, b, dims, precision=prec, preferred_element_type=F32)


def _bdot(a, b, dims=NN):
    return _dot(a.astype(BF16), b.astype(BF16), dims)


def _sig(x):
    return 1.0 / (1.0 + jnp.exp(-x))


class _Carry:
    def __init__(self, ins, outs, aliases, scratch, phases):
        self.ins, self.outs, self.aliases, self.scratch, self.phases = ins, outs, aliases, scratch, phases


def _in_hbm(args):
    return [pltpu.with_memory_space_constraint(a, pltpu.HBM) for a in args]


def _out_hbm(shapes):
    if isinstance(shapes, (list, tuple)):
        return [pltpu.HBM(s.shape, s.dtype) for s in shapes]
    return pltpu.HBM(shapes.shape, shapes.dtype)


def _carry_join(a, b):
    na_in, na_out, na_sc = len(a.ins), len(a.outs), len(a.scratch)
    aliases = dict(a.aliases)
    aliases.update({na_in + i: na_out + o for i, o in b.aliases.items()})

    def phases(ins, outs, sems):
        pa = a.phases(ins[:na_in], outs[:na_out], sems[:na_sc])
        pb = b.phases(ins[na_in:], outs[na_out:], sems[na_sc:])

        def both(fa, fb):
            if fa is None and fb is None:
                return None

            def run():
                for fn in (fa, fb):
                    if fn is not None:
                        fn()
            return run

        return tuple(both(fa, fb) for fa, fb in zip(pa, pb))

    return _Carry(list(a.ins) + list(b.ins), list(a.outs) + list(b.outs), aliases,
                  list(a.scratch) + list(b.scratch), phases)


def _pcall(body, *, name, grid, in_specs, out_specs, out_shape, scratch=(), aliases=None, vmem_mb=48,
           carry=None):
    params = pltpu.CompilerParams(dimension_semantics=("arbitrary",) * len(grid),
                                  vmem_limit_bytes=vmem_mb << 20)
    if carry is None:
        plain = pl.pallas_call(
            body, name=name, grid=grid, in_specs=in_specs, out_specs=out_specs, out_shape=_out_hbm(out_shape),
            scratch_shapes=list(scratch), input_output_aliases=aliases or {}, compiler_params=params)
        return lambda *args: plain(*_in_hbm(args))
    single = not isinstance(out_shape, (list, tuple))
    out_specs_l = [out_specs] if single else list(out_specs)
    out_shape_l = [out_shape] if single else list(out_shape)
    n_in, n_out, n_sc = len(in_specs), len(out_shape_l), len(scratch)
    k_in, k_out = len(carry.ins), len(carry.outs)
    nsteps = int(np.prod(grid))
    assert nsteps >= 3

    def wrapped(*refs):
        ins, cins = refs[:n_in], refs[n_in:n_in + k_in]
        o0 = n_in + k_in
        outs, couts = refs[o0:o0 + n_out], refs[o0 + n_out:o0 + n_out + k_out]
        s0 = o0 + n_out + k_out
        sc, csc = refs[s0:s0 + n_sc], refs[s0 + n_sc:]
        step = pl.program_id(0)
        for ax in range(1, len(grid)):
            step = step * grid[ax] + pl.program_id(ax)
        start, mid, end = carry.phases(cins, couts, csc)
        pl.when(step == 0)(start)
        body(*ins, *outs, *sc)
        if mid is not None:
            pl.when(step == nsteps - 2)(mid)
        pl.when(step == nsteps - 1)(end)

    all_aliases = dict(aliases or {})
    all_aliases.update({n_in + i: n_out + o for i, o in carry.aliases.items()})
    call = pl.pallas_call(
        wrapped, name=name, grid=grid, in_specs=list(in_specs) + [ANY] * k_in,
        out_specs=out_specs_l + [ANY] * k_out, out_shape=_out_hbm(out_shape_l + list(carry.outs)),
        scratch_shapes=list(scratch) + list(carry.scratch), input_output_aliases=all_aliases,
        compiler_params=params)

    def run(*args):
        res = call(*_in_hbm(args), *carry.ins)
        core = res[:n_out]
        return (core[0] if single else list(core)), list(res[n_out:])

    return run


def _full(shape):
    nd = len(shape)
    return pl.BlockSpec(shape, lambda *_: (0,) * nd)


ANY = pl.BlockSpec(memory_space=pl.ANY)


def _mm(a, b, *, name, mode="nn", out_dtype=F32, tm, tn, tk):
    if mode == "nn":
        (m, k), (k2, n) = a.shape, b.shape
    elif mode == "nt":
        (m, k), (n, k2) = a.shape, b.shape
    else:
        (k, m), (k2, n) = a.shape, b.shape
    assert k == k2 and m % tm == 0 and n % tn == 0 and k % tk == 0, (name, a.shape, b.shape)
    nk = k // tk
    dims = {"nn": NN, "nt": NT, "tn": TN}[mode]

    def body(a_ref, b_ref, o_ref, acc):
        kk = pl.program_id(2)

        @pl.when(kk == 0)
        def _():
            acc[...] = jnp.zeros_like(acc)

        acc[...] += _bdot(a_ref[...], b_ref[...], dims)

        @pl.when(kk == nk - 1)
        def _():
            o_ref[...] = acc[...].astype(out_dtype)

    a_spec = (pl.BlockSpec((tk, tm), lambda i, j, kk: (kk, i)) if mode == "tn"
              else pl.BlockSpec((tm, tk), lambda i, j, kk: (i, kk)))
    b_spec = (pl.BlockSpec((tn, tk), lambda i, j, kk: (j, kk)) if mode == "nt"
              else pl.BlockSpec((tk, tn), lambda i, j, kk: (kk, j)))
    return _pcall(body, name=name, grid=(m // tm, n // tn, nk), in_specs=[a_spec, b_spec],
                  out_specs=pl.BlockSpec((tm, tn), lambda i, j, kk: (i, j)),
                  out_shape=jax.ShapeDtypeStruct((m, n), out_dtype),
                  scratch=[pltpu.VMEM((tm, tn), F32)])(a, b)


NT_IN = NCOL // 256


def _src_block(j):
    return j + jnp.where(j < 4, 2, jnp.where(j < 6, 3, jnp.where(j < 8, -6, jnp.where(
        j < 16, 5, jnp.where(j < 20, -7, -14)))))


def _mm_in(h, wt, tm, carry=None):
    tt = h.shape[0]

    def body(h_ref, w_ref, o_ref):
        o_ref[...] = _bdot(h_ref[...], w_ref[...], NT)

    return _pcall(body, name="mm_in", grid=(tt // tm, NT_IN),
                  in_specs=[pl.BlockSpec((tm, D), lambda i, j: (i, 0)),
                            pl.BlockSpec((256, D), lambda i, j: (_src_block(j), 0))],
                  out_specs=pl.BlockSpec((tm, 256), lambda i, j: (i, j)),
                  out_shape=jax.ShapeDtypeStruct((tt, NCOL), F32), carry=carry)(h, wt)


def _mm_dh(dp, wt, tm, carry=None):
    tt = dp.shape[0]
    per, ng = 3, NT_IN // 3

    def body(d_ref, w0, w1, w2, o_ref, acc):
        kk = pl.program_id(1)

        @pl.when(kk == 0)
        def _():
            acc[...] = jnp.zeros_like(acc)

        acc[...] += (_bdot(d_ref[:, 0:256], w0[...]) + _bdot(d_ref[:, 256:512], w1[...])
                     + _bdot(d_ref[:, 512:768], w2[...]))

        @pl.when(kk == ng - 1)
        def _():
            o_ref[...] = acc[...]

    wspecs = [pl.BlockSpec((256, D), functools.partial(lambda t, i, kk: (_src_block(per * kk + t), 0), t))
              for t in range(per)]
    return _pcall(body, name="mm_dh", grid=(tt // tm, ng),
                  in_specs=[pl.BlockSpec((tm, per * 256), lambda i, kk: (i, kk))] + wspecs,
                  out_specs=pl.BlockSpec((tm, D), lambda i, kk: (i, 0)),
                  out_shape=jax.ShapeDtypeStruct((tt, D), F32), scratch=[pltpu.VMEM((tm, D), F32)],
                  carry=carry)(dp, wt, wt, wt)


def _mm_gin(dp, h, tk):
    tt = dp.shape[0]
    nk = tt // tk

    def body(d_ref, h_ref, o_ref, acc):
        kk = pl.program_id(1)

        @pl.when(kk == 0)
        def _():
            acc[...] = jnp.zeros_like(acc)

        acc[...] += _bdot(d_ref[...], h_ref[...], TN)

        @pl.when(kk == nk - 1)
        def _():
            o_ref[...] = acc[...]

    return _pcall(body, name="mm_gin", grid=(NT_IN, nk),
                  in_specs=[pl.BlockSpec((tk, 256), lambda j, kk: (kk, j)),
                            pl.BlockSpec((tk, D), lambda j, kk: (kk, 0))],
                  out_specs=pl.BlockSpec((256, D), lambda j, kk: (_src_block(j), 0)),
                  out_shape=jax.ShapeDtypeStruct((NCOL, D), F32), scratch=[pltpu.VMEM((256, D), F32)])(dp, h)


def _tok_specs():
    assert L == TM
    return [_full((TM, D)), pl.BlockSpec((TM, D), lambda i: (jnp.maximum(i - 1, 0), 0))]


def _mod1(ctx, x, nw, ss):
    rows = L + x.shape[0]

    def body(c_ref, x_ref, nw_ref, ss_ref, h_ref):
        t = jnp.where(pl.program_id(0) == 0, c_ref[...], x_ref[...])
        r = lax.rsqrt(jnp.mean(t * t, axis=-1, keepdims=True) + EPS)
        s = ss_ref[0]
        h_ref[...] = ((t * r * nw_ref[...]) * (1.0 + s[1:2]) + s[0:1]).astype(BF16)

    return _pcall(body, name="mod1", grid=(rows // TM,),
                  in_specs=_tok_specs() + [_full((1, D)),
                                           pl.BlockSpec((1, 2, D), lambda i: (jnp.minimum(i, 1), 0, 0))],
                  out_specs=pl.BlockSpec((TM, D), lambda i: (i, 0)),
                  out_shape=jax.ShapeDtypeStruct((rows, D), BF16))(ctx, x, nw, ss)


def _norm_bwd_rows(x, dh, nw, scale):
    r = lax.rsqrt(jnp.mean(x * x, axis=-1, keepdims=True) + EPS)
    xh = x * r
    dxh = dh * ((1.0 + scale) * nw)
    dx = r * (dxh - xh * jnp.mean(dxh * xh, axis=-1, keepdims=True))
    return dx, xh


def _out_proj_mod2(mixed, w_o, x, g1, nw2, ss2):
    s_len = x.shape[0]
    tm = 512

    def body(m_ref, w_ref, x_ref, g_ref, nw_ref, ss_ref, ao_ref, x1_ref, h_ref):
        ao = _bdot(m_ref[...], w_ref[...])
        ao_ref[...] = ao
        x1 = x_ref[...] + g_ref[...] * ao
        x1_ref[...] = x1
        r = lax.rsqrt(jnp.mean(x1 * x1, axis=-1, keepdims=True) + EPS)
        s = ss_ref[0]
        h_ref[...] = ((x1 * r * nw_ref[...]) * (1.0 + s[1:2]) + s[0:1]).astype(BF16)

    row = pl.BlockSpec((tm, D), lambda i: (i, 0))
    f = jax.ShapeDtypeStruct((s_len, D), F32)
    return _pcall(body, name="out_proj_mod2", grid=(s_len // tm,),
                  in_specs=[row, _full((D, D)), row, _full((1, D)), _full((1, D)), _full((1, 2, D))],
                  out_specs=[row, row, row],
                  out_shape=[f, f, jax.ShapeDtypeStruct((s_len, D), BF16)])(mixed, w_o, x, g1, nw2, ss2)


TS = 1024


def _acc_call(body, *, name, grid, in_specs, out_specs, out_shape, acc_shapes, args, carry=None):
    return _pcall(body, name=name, grid=grid, in_specs=in_specs, out_specs=out_specs, out_shape=out_shape,
                  scratch=[pltpu.VMEM(s, F32) for s in acc_shapes], carry=carry)(*args)


def _mm_cs(a, w4, *, name):
    m, k = a.shape
    _, _, ns = w4.shape

    def body(a_ref, w_ref, o_ref):
        o_ref[...] = _bdot(a_ref[...], w_ref[0])

    return _pcall(body, name=name, grid=(m // TS, 4),
                  in_specs=[pl.BlockSpec((TS, k), lambda i, j: (i, 0)),
                            pl.BlockSpec((1, k, ns), lambda i, j: (j, 0, 0))],
                  out_specs=pl.BlockSpec((TS, ns), lambda i, j: (i, j)),
                  out_shape=jax.ShapeDtypeStruct((m, 4 * ns), F32))(a, w4)


def _mm_cs_nt(a, w4, *, name):
    m = a.shape[0]
    _, k, ns = w4.shape

    def body(a_ref, w_ref, o_ref, acc):
        j = pl.program_id(1)

        @pl.when(j == 0)
        def _():
            acc[...] = jnp.zeros_like(acc)

        acc[...] += _bdot(a_ref[...], w_ref[0], NT)

        @pl.when(j == 3)
        def _():
            o_ref[...] = acc[...]

    return _acc_call(body, name=name, grid=(m // TS, 4),
                     in_specs=[pl.BlockSpec((TS, ns), lambda i, j: (i, j)),
                               pl.BlockSpec((1, k, ns), lambda i, j: (j, 0, 0))],
                     out_specs=pl.BlockSpec((TS, k), lambda i, j: (i, 0)),
                     out_shape=jax.ShapeDtypeStruct((m, k), F32), acc_shapes=[(TS, k)], args=(a, w4))


def _mm_cs_tn(a, b, ns, *, name):
    s_len, k = a.shape
    nk = s_len // TS

    def body(a_ref, b_ref, o_ref, acc):
        t = pl.program_id(1)

        @pl.when(t == 0)
        def _():
            acc[...] = jnp.zeros_like(acc)

        acc[...] += _bdot(a_ref[...], b_ref[...], TN)

        @pl.when(t == nk - 1)
        def _():
            o_ref[0] = acc[...]

    return _acc_call(body, name=name, grid=(4, nk),
                     in_specs=[pl.BlockSpec((TS, k), lambda j, t: (t, 0)),
                               pl.BlockSpec((TS, ns), lambda j, t: (t, j))],
                     out_specs=pl.BlockSpec((1, k, ns), lambda j, t: (j, 0, 0)),
                     out_shape=jax.ShapeDtypeStruct((4, k, ns), F32), acc_shapes=[(k, ns)], args=(a, b))


def _ffn_up(h2, g4, u4, carry=None):
    s_len = h2.shape[0]
    ns = g4.shape[1]

    def body(h_ref, g_ref, u_ref, a_ref, b_ref, z_ref):
        h = h_ref[...]
        a = _bdot(h, g_ref[0], NT)
        b = _bdot(h, u_ref[0], NT)
        a_ref[0] = a.astype(BF16)
        b_ref[0] = b.astype(BF16)
        z_ref[0] = (a * _sig(a) * b).astype(BF16)

    w = pl.BlockSpec((1, ns, D), lambda i, j: (j, 0, 0))
    o = pl.BlockSpec((1, TS, ns), lambda i, j: (j, i, 0))
    f = jax.ShapeDtypeStruct((4, s_len, ns), BF16)
    return _pcall(body, name="ffn_up", grid=(s_len // TS, 4),
                  in_specs=[pl.BlockSpec((TS, D), lambda i, j: (i, 0)), w, w], out_specs=[o, o, o],
                  out_shape=[f, f, jax.ShapeDtypeStruct((4, s_len, ns), BF16)], carry=carry)(h2, g4, u4)


def _ffn_down_loss(z4, dn4, x1, g2, tgt):
    _, s_len, ns = z4.shape

    def body(z_ref, w_ref, x1_ref, g_ref, t_ref, sq_ref, dx2_ref, dyb_ref, dg_ref, acc):
        i, j = pl.program_id(0), pl.program_id(1)

        @pl.when((i == 0) & (j == 0))
        def _():
            sq_ref[...] = jnp.zeros_like(sq_ref)
            dg_ref[...] = jnp.zeros_like(dg_ref)

        @pl.when(j == 0)
        def _():
            acc[...] = jnp.zeros_like(acc)

        acc[...] += _bdot(z_ref[0], w_ref[0])

        @pl.when(j == 3)
        def _():
            y_ = acc[...]
            g = g_ref[...]
            e = x1_ref[...] + g * y_ - t_ref[...]
            sq_ref[...] += jnp.sum(e * e, axis=0, keepdims=True)
            dx2 = e * (1.0 / D)
            dx2_ref[...] = dx2
            dyb_ref[...] = (g * dx2).astype(BF16)
            dg_ref[...] += jnp.sum(dx2 * y_, axis=0, keepdims=True)

    row = pl.BlockSpec((TS, D), lambda i, j: (i, 0))
    vec = _full((1, D))
    return _acc_call(body, name="ffn_down_loss", grid=(s_len // TS, 4),
                     in_specs=[pl.BlockSpec((1, TS, ns), lambda i, j: (j, i, 0)),
                               pl.BlockSpec((1, ns, D), lambda i, j: (j, 0, 0)), row, vec, row],
                     out_specs=[vec, row, row, vec],
                     out_shape=[jax.ShapeDtypeStruct((1, D), F32), jax.ShapeDtypeStruct((s_len, D), F32),
                                jax.ShapeDtypeStruct((s_len, D), BF16), jax.ShapeDtypeStruct((1, D), F32)],
                     acc_shapes=[(TS, D)], args=(z4, dn4, x1, g2, tgt))


def _ffn_dz(dyb, dn4, a4, b4):
    _, s_len, ns = a4.shape

    def body(dy_ref, w_ref, a_ref, b_ref, da_ref, db_ref):
        dz = _bdot(dy_ref[...], w_ref[0], NT)
        a = a_ref[0].astype(F32)
        s = _sig(a)
        da_ref[0] = (dz * b_ref[0].astype(F32) * (s * (1.0 + a * (1.0 - s)))).astype(BF16)
        db_ref[0] = (dz * (a * s)).astype(BF16)

    t = pl.BlockSpec((1, TS, ns), lambda i, j: (j, i, 0))
    o = jax.ShapeDtypeStruct((4, s_len, ns), BF16)
    return _pcall(body, name="ffn_dz", grid=(s_len // TS, 4),
                  in_specs=[pl.BlockSpec((TS, D), lambda i, j: (i, 0)),
                            pl.BlockSpec((1, ns, D), lambda i, j: (j, 0, 0)), t, t],
                  out_specs=[t, t], out_shape=[o, o])(dyb, dn4, a4, b4)


def _ffn_gdn(z4, dyb):
    _, s_len, ns = z4.shape
    nk = s_len // TS

    def body(z_ref, dy_ref, o_ref, acc):
        t = pl.program_id(1)

        @pl.when(t == 0)
        def _():
            acc[...] = jnp.zeros_like(acc)

        acc[...] += _bdot(z_ref[0], dy_ref[...], TN)

        @pl.when(t == nk - 1)
        def _():
            o_ref[0] = acc[...]

    return _acc_call(body, name="ffn_gdn", grid=(4, nk),
                     in_specs=[pl.BlockSpec((1, TS, ns), lambda j, t: (j, t, 0)),
                               pl.BlockSpec((TS, D), lambda j, t: (t, 0))],
                     out_specs=pl.BlockSpec((1, ns, D), lambda j, t: (j, 0, 0)),
                     out_shape=jax.ShapeDtypeStruct((4, ns, D), F32), acc_shapes=[(ns, D)], args=(z4, dyb))


def _ffn_dh2(da4, db4, g4, u4, carry=None):
    _, s_len, ns = da4.shape

    def body(da_ref, db_ref, g_ref, u_ref, o_ref, acc):
        j = pl.program_id(1)

        @pl.when(j == 0)
        def _():
            acc[...] = jnp.zeros_like(acc)

        acc[...] += _bdot(da_ref[0], g_ref[0]) + _bdot(db_ref[0], u_ref[0])

        @pl.when(j == 3)
        def _():
            o_ref[...] = acc[...]

    t = pl.BlockSpec((1, TS, ns), lambda i, j: (j, i, 0))
    w = pl.BlockSpec((1, ns, D), lambda i, j: (j, 0, 0))
    return _acc_call(body, name="ffn_dh2", grid=(s_len // TS, 4), in_specs=[t, t, w, w],
                     out_specs=pl.BlockSpec((TS, D), lambda i, j: (i, 0)),
                     out_shape=jax.ShapeDtypeStruct((s_len, D), F32), acc_shapes=[(TS, D)],
                     args=(da4, db4, g4, u4), carry=carry)


def _ffn_ggu(h2, da4, db4, carry=None):
    _, s_len, ns = da4.shape
    nk = s_len // TS

    def body(h_ref, da_ref, db_ref, gg_ref, gu_ref, acc_g, acc_u):
        t = pl.program_id(1)

        @pl.when(t == 0)
        def _():
            acc_g[...] = jnp.zeros_like(acc_g)
            acc_u[...] = jnp.zeros_like(acc_u)

        h = h_ref[...]
        acc_g[...] += _bdot(da_ref[0], h, TN)
        acc_u[...] += _bdot(db_ref[0], h, TN)

        @pl.when(t == nk - 1)
        def _():
            gg_ref[0] = acc_g[...]
            gu_ref[0] = acc_u[...]

    d = pl.BlockSpec((1, TS, ns), lambda j, t: (j, t, 0))
    o = pl.BlockSpec((1, ns, D), lambda j, t: (j, 0, 0))
    f = jax.ShapeDtypeStruct((4, ns, D), F32)
    return _acc_call(body, name="ffn_ggu", grid=(4, nk),
                     in_specs=[pl.BlockSpec((TS, D), lambda j, t: (t, 0)), d, d], out_specs=[o, o],
                     out_shape=[f, f], acc_shapes=[(ns, D), (ns, D)], args=(h2, da4, db4), carry=carry)


def _mod2_bwd(x1, dh2, dx2, ao, nw2, ss2, g1):
    s_len = x1.shape[0]

    def body(x1_ref, dh_ref, dx2_ref, ao_ref, nw_ref, ss_ref, g_ref,
             dx1_ref, da_ref, dss_ref, dnw_ref, dg_ref):
        i = pl.program_id(0)

        @pl.when(i == 0)
        def _():
            dss_ref[...] = jnp.zeros_like(dss_ref)
            dnw_ref[...] = jnp.zeros_like(dnw_ref)
            dg_ref[...] = jnp.zeros_like(dg_ref)

        dh = dh_ref[...]
        nw = nw_ref[...]
        scale = ss_ref[0][1:2]
        dxn, xh = _norm_bwd_rows(x1_ref[...], dh, nw, scale)
        dx1 = dx2_ref[...] + dxn
        dx1_ref[...] = dx1
        da_ref[...] = (g_ref[...] * dx1).astype(BF16)
        dg_ref[...] += jnp.sum(dx1 * ao_ref[...], axis=0, keepdims=True)
        dsh = jnp.sum(dh, axis=0, keepdims=True)
        dsc = jnp.sum(dh * xh * nw, axis=0, keepdims=True)
        dss_ref[...] += jnp.concatenate([dsh, dsc], axis=0)
        dnw_ref[...] += jnp.sum(dh * xh * (1.0 + scale), axis=0, keepdims=True)

    row = pl.BlockSpec((TM, D), lambda i: (i, 0))
    vec = _full((1, D))
    return _pcall(body, name="mod2_bwd", grid=(s_len // TM,),
                  in_specs=[row, row, row, row, vec, _full((1, 2, D)), vec],
                  out_specs=[row, row, _full((2, D)), vec, vec],
                  out_shape=[jax.ShapeDtypeStruct((s_len, D), F32), jax.ShapeDtypeStruct((s_len, D), BF16),
                             jax.ShapeDtypeStruct((2, D), F32), jax.ShapeDtypeStruct((1, D), F32),
                             jax.ShapeDtypeStruct((1, D), F32)])(x1, dh2, dx2, ao, nw2, ss2, g1)


def _mod1_bwd(ctx, x, dh, dx1, nw1, ss1, carry=None):
    s_len = dx1.shape[0]
    tt = L + s_len

    def body(c_ref, x_ref, dh_ref, dx1_ref, nw_ref, ss_ref, dx_ref, dss_ref, dnw_ref):
        i = pl.program_id(0)
        tok = jnp.where(i == 0, c_ref[...], x_ref[...])

        @pl.when(i == 0)
        def _():
            dnw_ref[...] = jnp.zeros_like(dnw_ref)

        @pl.when(i <= 1)
        def _():
            dss_ref[...] = jnp.zeros_like(dss_ref)

        dh_ = dh_ref[...]
        nw = nw_ref[...]
        scale = ss_ref[0][1:2]
        dxn, xh = _norm_bwd_rows(tok, dh_, nw, scale)

        @pl.when(i >= 1)
        def _():
            dx_ref[...] = dx1_ref[...] + dxn

        dsh = jnp.sum(dh_, axis=0, keepdims=True)
        dsc = jnp.sum(dh_ * xh * nw, axis=0, keepdims=True)
        dss_ref[...] += jnp.concatenate([dsh, dsc], axis=0)[None]
        dnw_ref[...] += jnp.sum(dh_ * xh * (1.0 + scale), axis=0, keepdims=True)

    row = pl.BlockSpec((TM, D), lambda i: (i, 0))
    lat = pl.BlockSpec((TM, D), lambda i: (jnp.maximum(i - 1, 0), 0))
    sel = pl.BlockSpec((1, 2, D), lambda i: (jnp.minimum(i, 1), 0, 0))
    return _pcall(body, name="mod1_bwd", grid=(tt // TM,),
                  in_specs=_tok_specs() + [row, lat, _full((1, D)), sel],
                  out_specs=[lat, sel, _full((1, D))],
                  out_shape=[jax.ShapeDtypeStruct((s_len, D), F32), jax.ShapeDtypeStruct((2, 2, D), F32),
                             jax.ShapeDtypeStruct((1, D), F32)], carry=carry)(ctx, x, dh, dx1, nw1, ss1)


def _rows(c):
    return slice(c * CH, (c + 1) * CH)


def _chunk_masks(rev, transpose=False):
    r = lax.broadcasted_iota(jnp.int32, (TM, TM), 0)
    c = lax.broadcasted_iota(jnp.int32, (TM, TM), 1)
    same = (r // CH) == (c // CH)
    before = (c >= r) if (rev != transpose) else (c <= r)
    return same & before, same


def _chunk_scan(x, rev, transpose=False):
    r = lax.broadcasted_iota(jnp.int32, (CH, CH), 0)
    c = lax.broadcasted_iota(jnp.int32, (CH, CH), 1)
    tri = ((c >= r) if (rev != transpose) else (c <= r)).astype(F32)
    return jnp.concatenate([_dot(tri, x[_rows(ch)], prec=HI) for ch in range(x.shape[0] // CH)], axis=0)


def _chunk_total(x):
    return jnp.concatenate([jnp.broadcast_to(jnp.sum(x[_rows(ch)], axis=0, keepdims=True), (CH, x.shape[1]))
                            for ch in range(x.shape[0] // CH)], axis=0)


def _hgrn_gate(fl, qraw, lg):
    lb = 1.0 / (1.0 + jnp.exp(lg[1:2] - lg[0:1]))
    sg = _sig(fl)
    f = lb + (1.0 - lb) * sg
    q = qraw * _sig(qraw) * (HGD ** -0.5)
    return lb, sg, f, q


def _hgrn_fwd(p, lg, *, rev, carry=None):
    tt = p.shape[0]
    nt = tt // TM
    ncht = TM // CH
    d = 1 if rev else 0

    def tile_of(s):
        return jnp.where(s == 0, 0, nt - s) if rev else s

    def body(f_ref, inp_ref, q_ref, lg_ref, o_ref, st_ref, state):
        s = pl.program_id(0)

        @pl.when(s == 0)
        def _():
            state[...] = jnp.zeros_like(state)

        _, _, f, q = _hgrn_gate(f_ref[...], q_ref[...], lg_ref[0])
        lf = jnp.log(f)
        causal, _ = _chunk_masks(rev)
        cum = _chunk_scan(lf, rev)
        tot = _chunk_total(lf)
        qd = (q * jnp.exp(cum)).astype(BF16)
        kd = ((1.0 - f) * jnp.exp(-cum)).astype(BF16)
        ke = ((1.0 - f) * jnp.exp(tot - cum)).astype(BF16)
        et = jnp.exp(tot)
        v = inp_ref[...].astype(BF16)
        order = range(ncht - 1, -1, -1) if rev else range(ncht)
        outs = []
        for h in range(4):
            sl = slice(h * HGD, (h + 1) * HGD)
            qd_, kd_, ke_, v_ = qd[:, sl], kd[:, sl], ke[:, sl], v[:, sl]
            pm = jnp.where(causal, _dot(qd_, kd_, NT), 0.0).astype(BF16)
            o_h = _dot(pm, v_)
            upd = [_dot(v_[_rows(c)], ke_[_rows(c)], TN) for c in range(ncht)]
            st = state[h]
            for c in order:
                st_ref[c, h] = st
                st = st * et[c * CH:c * CH + 1, sl] + upd[c]
            state[h] = st
            inter = [_dot(qd_[_rows(c)], st_ref[c, h].astype(BF16), NT) for c in range(ncht)]
            outs.append(o_h + jnp.concatenate(inter, axis=0))
        o_ref[...] = jnp.concatenate(outs, axis=1)

    def col(cb):
        return pl.BlockSpec((TM, HGW), lambda s: (tile_of(s), cb))

    return _pcall(
        body, name="hgrn_fwd_rev" if rev else "hgrn_fwd", grid=(nt,),
        in_specs=[col(C_FB if rev else C_FF), col(C_INP), col(C_QHG),
                  pl.BlockSpec((1, 2, HGW), lambda s: (d, 0, 0))],
        out_specs=[pl.BlockSpec((TM, HGW), lambda s: (tile_of(s), 0)),
                   pl.BlockSpec((ncht, 4, HGD, HGD), lambda s: (tile_of(s), 0, 0, 0))],
        out_shape=[jax.ShapeDtypeStruct((tt, HGW), F32),
                   jax.ShapeDtypeStruct((nt * ncht, 4, HGD, HGD), F32)],
        scratch=[pltpu.VMEM((4, HGD, HGD), F32)], carry=carry)(p, p, p, lg)


def _hgrn_bwd(p, lg, do, st, dp, prev, *, rev, carry=None):
    tt = p.shape[0]
    nt = tt // TM
    ncht = TM // CH
    d = 1 if rev else 0
    second = prev is not None

    def tile_of(s):
        return jnp.where(s == nt - 1, 0, s + 1) if rev else nt - 1 - s

    def body(*refs):
        if second:
            (f_ref, inp_ref, q_ref, lg_ref, do_ref, st_ref, dvp_ref, dqp_ref, _dp_in,
             dp_ref, dlg_ref, dstate) = refs
        else:
            (f_ref, inp_ref, q_ref, lg_ref, do_ref, st_ref, _dp_in,
             dp_ref, dv_ref, dq_ref, dlg_ref, dstate) = refs
        s = pl.program_id(0)
        tile = tile_of(s)

        @pl.when(s == 0)
        def _():
            dstate[...] = jnp.zeros_like(dstate)
            dlg_ref[...] = jnp.zeros_like(dlg_ref)

        qraw = q_ref[...]
        lb, sg, f, q = _hgrn_gate(f_ref[...], qraw, lg_ref[0])
        lf = jnp.log(f)
        causal, _ = _chunk_masks(rev)
        causal_t, _ = _chunk_masks(rev, transpose=True)
        cum = _chunk_scan(lf, rev)
        tot = _chunk_total(lf)
        ea, eb, ee, et = jnp.exp(cum), jnp.exp(-cum), jnp.exp(tot - cum), jnp.exp(tot)
        qdf, kdf, kef = q * ea, (1.0 - f) * eb, (1.0 - f) * ee
        qd, kd, ke = qdf.astype(BF16), kdf.astype(BF16), kef.astype(BF16)
        v = inp_ref[...].astype(BF16)
        dob = jnp.where(tile == 0, 0.0, do_ref[...]).astype(BF16)
        order = range(ncht) if rev else range(ncht - 1, -1, -1)
        dq_l, dk_l, dv_l, dcum_l, dtot_l = [], [], [], [], []
        for h in range(4):
            sl = slice(h * HGD, (h + 1) * HGD)
            qd_, kd_, ke_, v_, do_ = qd[:, sl], kd[:, sl], ke[:, sl], v[:, sl], dob[:, sl]
            pmt = jnp.where(causal_t, _dot(kd_, qd_, NT), 0.0).astype(BF16)
            dpm = jnp.where(causal, _dot(do_, v_, NT), 0.0).astype(BF16)
            dpmt = jnp.where(causal_t, _dot(v_, do_, NT), 0.0).astype(BF16)
            dv = _dot(pmt, do_)
            dqd = _dot(dpm, kd_)
            dkd = _dot(dpmt, qd_)
            upd = [_dot(do_[_rows(c)], qd_[_rows(c)], TN) for c in range(ncht)]
            ds = dstate[h]
            ds1 = [None] * ncht
            for c in order:
                ds1[c] = ds
                ds = ds * et[c * CH:c * CH + 1, sl] + upd[c]
            dstate[h] = ds
            dke_c, dv_c, dqd_c, dtot_c = [], [], [], []
            for c in range(ncht):
                st0 = st_ref[c, h]
                dsb = ds1[c].astype(BF16)
                dke_ = _dot(v_[_rows(c)], dsb)
                dke_c.append(dke_)
                dv_c.append(_dot(ke_[_rows(c)], dsb, NT))
                dqd_c.append(_dot(do_[_rows(c)], st0.astype(BF16)))
                dt = (jnp.sum(ds1[c] * st0, axis=0, keepdims=True) * et[c * CH:c * CH + 1, sl]
                      + jnp.sum(dke_ * kef[_rows(c), sl], axis=0, keepdims=True))
                dtot_c.append(jnp.broadcast_to(dt, (CH, HGD)))
            dke = jnp.concatenate(dke_c, axis=0)
            dqd = dqd + jnp.concatenate(dqd_c, axis=0)
            dv_l.append(dv + jnp.concatenate(dv_c, axis=0))
            dtot_l.append(jnp.concatenate(dtot_c, axis=0))
            dq_l.append(dqd * ea[:, sl])
            dk_l.append(dkd * eb[:, sl] + dke * ee[:, sl])
            dcum_l.append(dqd * qdf[:, sl] - dkd * kdf[:, sl] - dke * kef[:, sl])
        dcum = jnp.concatenate(dcum_l, axis=1)
        dlf = _chunk_scan(dcum, rev, transpose=True) + jnp.concatenate(dtot_l, axis=1)
        dq_t = jnp.concatenate(dq_l, axis=1)
        dv_t = jnp.concatenate(dv_l, axis=1)

        df = dlf / f - jnp.concatenate(dk_l, axis=1)
        dfl = df * (1.0 - lb) * sg * (1.0 - sg)
        dlb = jnp.sum(df * (1.0 - sg), axis=0, keepdims=True)
        dl0 = dlb * lb * (1.0 - lb)
        dlg_ref[...] += jnp.concatenate([dl0, -dl0], axis=0)[None]
        if second:
            sq = _sig(qraw)
            dqr = (dqp_ref[...] + dq_t) * (HGD ** -0.5) * (sq * (1.0 + qraw * (1.0 - sq)))
            dp_ref[...] = jnp.concatenate([dfl, dvp_ref[...] + dv_t, dqr], axis=1).astype(BF16)
        else:
            dp_ref[...] = dfl.astype(BF16)
            dv_ref[...] = dv_t
            dq_ref[...] = dq_t

    def col(cb):
        return pl.BlockSpec((TM, HGW), lambda s: (tile_of(s), cb))

    tok = pl.BlockSpec((TM, HGW), lambda s: (tile_of(s), 0))
    in_specs = [col(C_FB if rev else C_FF), col(C_INP), col(C_QHG),
                pl.BlockSpec((1, 2, HGW), lambda s: (d, 0, 0)),
                pl.BlockSpec((TM, HGW), lambda s: (jnp.maximum(tile_of(s) - 1, 0), 0)),
                pl.BlockSpec((ncht, 4, HGD, HGD), lambda s: (tile_of(s), 0, 0, 0))]
    args = [p, p, p, lg, do, st]
    dlg_spec = _full((1, 2, HGW))
    dlg_shape = jax.ShapeDtypeStruct((1, 2, HGW), F32)
    if second:
        in_specs += [tok, tok]
        args += [prev[0], prev[1]]
        out_specs = [pl.BlockSpec((TM, 3 * HGW), lambda s: (tile_of(s), 0)), dlg_spec]
        out_shape = [jax.ShapeDtypeStruct(dp.shape, BF16), dlg_shape]
    else:
        out_specs = [pl.BlockSpec((TM, HGW), lambda s: (tile_of(s), C_FB if rev else C_FF)), tok, tok, dlg_spec]
        out_shape = [jax.ShapeDtypeStruct(dp.shape, BF16), jax.ShapeDtypeStruct((tt, HGW), F32),
                     jax.ShapeDtypeStruct((tt, HGW), F32), dlg_shape]
    in_specs.append(ANY)
    args.append(dp)
    return _pcall(body, name="hgrn_bwd_rev" if rev else "hgrn_bwd", grid=(nt,),
                  in_specs=in_specs, out_specs=out_specs, out_shape=out_shape,
                  scratch=[pltpu.VMEM((4, HGD, HGD), F32)],
                  aliases={len(args) - 1: 0}, carry=carry)(*args)


def _head_rms(o, w, nheads):
    outs = []
    for h in range(nheads):
        oh = o[:, h * HGD:(h + 1) * HGD]
        outs.append(oh * lax.rsqrt(jnp.mean(oh * oh, axis=-1, keepdims=True) + EPS))
    return jnp.concatenate(outs, axis=1)


def _readout(o0, o1, p, hw4):
    s_len = o0.shape[0] - L

    def body(o0_ref, o1_ref, g_ref, w_ref, y_ref):
        xh = _head_rms(o0_ref[...] + o1_ref[...], None, 4)
        g = g_ref[...]
        y_ref[...] = (xh * w_ref[...] * (g * _sig(g))).astype(BF16)

    lat = pl.BlockSpec((TM, HGW), lambda i: (i + 1, 0))
    return _pcall(body, name="readout", grid=(s_len // TM,),
                  in_specs=[lat, lat, pl.BlockSpec((TM, HGW), lambda i: (i + 1, C_GHG)), _full((1, HGW))],
                  out_specs=pl.BlockSpec((TM, HGW), lambda i: (i, 0)),
                  out_shape=jax.ShapeDtypeStruct((s_len, HGW), BF16))(o0, o1, p, hw4)


def _readout_bwd(o0, o1, p, hw4, dy, dp, carry=None):
    tt = o0.shape[0]
    s_len = tt - L

    def body(o0_ref, o1_ref, g_ref, w_ref, dy_ref, _dp_in, dp_ref, do_ref, dw_ref):
        i = pl.program_id(0)

        @pl.when(i == 0)
        def _():
            dw_ref[...] = jnp.zeros_like(dw_ref)
            dp_ref[...] = jnp.zeros_like(dp_ref)

        @pl.when(i >= 1)
        def _():
            o = o0_ref[...] + o1_ref[...]
            g = g_ref[...]
            w = w_ref[...]
            sg = _sig(g)
            dy_ = dy_ref[...]
            dsw = dy_ * (g * sg)
            outs, xhs = [], []
            for h in range(4):
                sl = slice(h * HGD, (h + 1) * HGD)
                oh = o[:, sl]
                r = lax.rsqrt(jnp.mean(oh * oh, axis=-1, keepdims=True) + EPS)
                xh = oh * r
                dxh = dsw[:, sl] * w[:, sl]
                outs.append(r * (dxh - xh * jnp.mean(dxh * xh, axis=-1, keepdims=True)))
                xhs.append(xh)
            xh = jnp.concatenate(xhs, axis=1)
            do_ref[...] = jnp.concatenate(outs, axis=1)
            dp_ref[...] = (dy_ * xh * w * (sg * (1.0 + g * (1.0 - sg)))).astype(BF16)
            dw_ref[...] += jnp.sum(dsw * xh, axis=0, keepdims=True)

    tok = pl.BlockSpec((TM, HGW), lambda i: (i, 0))
    lat = pl.BlockSpec((TM, HGW), lambda i: (jnp.maximum(i - 1, 0), 0))
    return _pcall(body, name="readout_bwd", grid=(tt // TM,),
                  in_specs=[tok, tok, pl.BlockSpec((TM, HGW), lambda i: (i, C_GHG)), _full((1, HGW)), lat, ANY],
                  out_specs=[pl.BlockSpec((TM, HGW), lambda i: (i, C_GHG)), lat, _full((1, HGW))],
                  out_shape=[jax.ShapeDtypeStruct(dp.shape, BF16), jax.ShapeDtypeStruct((s_len, HGW), F32),
                             jax.ShapeDtypeStruct((1, HGW), F32)],
                  aliases={5: 0}, carry=carry)(o0, o1, p, hw4, dy, dp)


def _rope_tables(s_len):
    t = np.arange(s_len)
    inv = ROPE_THETA ** (-np.arange(0, 32, 2, dtype=np.float64) / 32)
    def half(pos):
        ang = pos[:, None].astype(np.float64) * inv[None, :]
        return (np.concatenate([np.cos(ang), np.cos(ang)], 1), np.concatenate([-np.sin(ang), np.sin(ang)], 1))
    cr, sr = half(t // GRID_W)
    cc, sc = half(t % GRID_W)
    cos = np.concatenate([cr, cc, cr, cc], 1)
    sin = np.concatenate([sr, sc, sr, sc], 1)
    cos = np.concatenate([np.ones((L, 128)), cos], 0)
    sin = np.concatenate([np.zeros((L, 128)), sin], 0)
    return jnp.asarray(cos, F32), jnp.asarray(sin, F32)


def _blockdiag(n, w):
    i = np.arange(n)
    return jnp.asarray((i[:, None] // w == i[None, :] // w) / float(w), F32)


def _dup_matrix():
    m = np.zeros((128, 512), np.float32)
    for g in range(2):
        for j in range(4):
            for dd in range(HDIM):
                m[64 * g + dd, 256 * g + 64 * j + dd] = 1.0
    return m


def _head_mean(x, blockdiag):
    return _dot(x, blockdiag, prec=lax.Precision.HIGH)


def _rot(x):
    n = x.shape[1]
    lane = lax.broadcasted_iota(jnp.int32, x.shape, 1)
    return jnp.where((lane % 32) < 16, pltpu.roll(x, n - 16, 1), pltpu.roll(x, 16, 1))


def _qk_prep(p, cos, sin, qnw8, knw2, bd512, bd128, dup):
    tt = p.shape[0]

    def body(q_ref, kv_ref, cos_ref, sin_ref, qw_ref, kw_ref, b5_ref, b1_ref, dup_ref,
             qr_ref, k4_ref, v4_ref):
        cos_, sin_ = cos_ref[...], sin_ref[...]
        q = q_ref[...]
        qn = q * lax.rsqrt(_head_mean(q * q, b5_ref[...]) + EPS) * qw_ref[...]
        cos4 = jnp.concatenate([cos_] * 4, axis=1)
        sin4 = jnp.concatenate([sin_] * 4, axis=1)
        qr_ref[...] = ((qn * cos4 + _rot(qn) * sin4) * (HDIM ** -0.5)).astype(BF16)
        kv = kv_ref[...]
        k, v = kv[:, :128], kv[:, 128:]
        kn = k * lax.rsqrt(_head_mean(k * k, b1_ref[...]) + EPS) * kw_ref[...]
        kr = kn * cos_ + _rot(kn) * sin_
        k4_ref[...] = _bdot(kr, dup_ref[...]).astype(BF16)
        v4_ref[...] = _bdot(v, dup_ref[...]).astype(BF16)

    row = lambda w, cb: pl.BlockSpec((TM, w), lambda i: (i, cb))
    out = jax.ShapeDtypeStruct((tt, ATW), BF16)
    return _pcall(body, name="qk_prep", grid=(tt // TM,),
                  in_specs=[row(ATW, C_QRAW), row(256, C_KV), row(128, 0), row(128, 0),
                            _full((1, ATW)), _full((1, 128)), _full((ATW, ATW)), _full((128, 128)),
                            _full((128, ATW))],
                  out_specs=[row(ATW, 0)] * 3, out_shape=[out] * 3)(
                      p, p, cos, sin, qnw8, knw2, bd512, bd128, dup)


def _attn_masks(i, nb):
    r = lax.broadcasted_iota(jnp.int32, (4 * BLK, 3 * BLK + L), 0) % BLK
    c = lax.broadcasted_iota(jnp.int32, (4 * BLK, 3 * BLK + L), 1)
    kpos = (i - 1) * BLK + c
    loc = (jnp.abs(c - BLK - r) <= BLK) & (kpos >= 0) & (kpos < nb * BLK)
    return loc | (c >= 3 * BLK)


def _stack_mask():
    r = lax.broadcasted_iota(jnp.int32, (4 * BLK, 256), 0)
    lane = lax.broadcasted_iota(jnp.int32, (4 * BLK, 256), 1)
    return (r // BLK) == (lane // HDIM)


def _stack_heads(xg, fill=0.0):
    x4 = jnp.concatenate([xg] * 4, axis=0)
    return jnp.where(_stack_mask(), x4, jnp.full_like(x4, fill))


def _unstack_heads(x4):
    out = jnp.where(_lane_mask(0), x4[0:BLK], 0.0)
    for j in range(1, 4):
        out = out + jnp.where(_lane_mask(j), x4[j * BLK:(j + 1) * BLK], 0.0)
    return out


def _per_head_rows(vals):
    return jnp.concatenate([jnp.broadcast_to(v, (BLK, 1)) for v in vals], axis=0)


def _lane_mask(j):
    lane = lax.broadcasted_iota(jnp.int32, (1, 256), 1)
    return (lane // HDIM) == j


def _attn_specs(nb):
    blk = lambda off: pl.BlockSpec((BLK, ATW), lambda i: (jnp.clip(i + off, 0, nb - 1) + 2, 0))
    ctx = pl.BlockSpec((L, ATW), lambda i: (0, 0))
    return blk, ctx


def _attn_fwd(qr, k4, v4, sinks, carry=None):
    tt = qr.shape[0]
    s_len = tt - L
    nb = s_len // BLK

    def body(sk_ref, q_ref, kp, ko, kn, kc, vp, vo, vn, vc, y_ref, lse_ref):
        i = pl.program_id(0)
        valid = _attn_masks(i, nb)
        q = q_ref[...]
        ys, lses = [], []
        for g in range(2):
            gs = slice(256 * g, 256 * g + 256)
            kcat = jnp.concatenate([kp[:, gs], ko[:, gs], kn[:, gs], kc[:, gs]], axis=0)
            vcat = jnp.concatenate([vp[:, gs], vo[:, gs], vn[:, gs], vc[:, gs]], axis=0)
            sink = _per_head_rows([sk_ref[4 * g + j] for j in range(4)])
            s = jnp.where(valid, _dot(_stack_heads(q[:, gs]), kcat, NT), -1e30)
            m = jnp.maximum(jnp.max(s, axis=-1, keepdims=True), sink)
            e = jnp.exp(s - m)
            den = jnp.sum(e, axis=-1, keepdims=True) + jnp.exp(sink - m)
            ys.append(_unstack_heads(_bdot(e * (1.0 / den), vcat)))
            lses.append(_unstack_heads(jnp.broadcast_to(m + jnp.log(den), (4 * BLK, 256))))
        y_ref[...] = jnp.concatenate(ys, axis=1).astype(BF16)
        lse_ref[...] = jnp.concatenate(lses, axis=1)

    blk, ctx = _attn_specs(nb)
    out = pl.BlockSpec((BLK, ATW), lambda i: (i, 0))
    return _pcall(body, name="attn_fwd", grid=(nb,),
                  in_specs=[pl.BlockSpec(memory_space=pltpu.SMEM), blk(0),
                            blk(-1), blk(0), blk(1), ctx, blk(-1), blk(0), blk(1), ctx],
                  out_specs=[out, out],
                  out_shape=[jax.ShapeDtypeStruct((s_len, ATW), BF16),
                             jax.ShapeDtypeStruct((s_len, ATW), F32)], carry=carry)(
                      sinks, qr, k4, k4, k4, k4, v4, v4, v4, v4)


def _attn_bwd(qr, k4, v4, sinks, y, lse, dy, carry=None):
    tt = qr.shape[0]
    s_len = tt - L
    nb = s_len // BLK

    def body(sk_ref, q_ref, kp, ko, kn, kc, vp, vo, vn, vc, y_ref, lse_ref, dy_ref,
             dq_ref, dkw_ref, dvw_ref, dkc_ref, dvc_ref, dsk_ref):
        i = pl.program_id(0)

        @pl.when(i == 0)
        def _():
            dkc_ref[...] = jnp.zeros_like(dkc_ref)
            dvc_ref[...] = jnp.zeros_like(dvc_ref)
            dsk_ref[...] = jnp.zeros_like(dsk_ref)

        valid = _attn_masks(i, nb)
        q = q_ref[...]
        dy_ = dy_ref[...]
        dly = dy_ * y_ref[...].astype(F32)
        lse_ = lse_ref[...]
        dqs = []
        for g in range(2):
            gs = slice(256 * g, 256 * g + 256)
            kcat = jnp.concatenate([kp[:, gs], ko[:, gs], kn[:, gs], kc[:, gs]], axis=0)
            vcat = jnp.concatenate([vp[:, gs], vo[:, gs], vn[:, gs], vc[:, gs]], axis=0)
            q4 = _stack_heads(q[:, gs])
            dy4 = _stack_heads(dy_[:, gs]).astype(BF16)
            lse4 = jnp.max(_stack_heads(lse_[:, gs], fill=-1e30), axis=-1, keepdims=True)
            delta = jnp.sum(_stack_heads(dly[:, gs]), axis=-1, keepdims=True)
            sink = _per_head_rows([sk_ref[4 * g + j] for j in range(4)])
            pr = jnp.where(valid, jnp.exp(_dot(q4, kcat, NT) - lse4), 0.0)
            dsb = (pr * (_dot(dy4, vcat, NT) - delta)).astype(BF16)
            dsink = jnp.exp(sink - lse4) * delta
            for j in range(4):
                dsk_ref[4 * g + j:4 * g + j + 1, :] += jnp.broadcast_to(
                    -jnp.sum(dsink[j * BLK:(j + 1) * BLK], axis=0, keepdims=True), (1, 128))
            dqs.append(_unstack_heads(_dot(dsb, kcat)))
            dkg = _dot(dsb, q4, TN)
            dvg = _dot(pr.astype(BF16), dy4, TN)
            dkw_ref[0, :, gs] = dkg[:3 * BLK]
            dvw_ref[0, :, gs] = dvg[:3 * BLK]
            dkc_ref[:, gs] += dkg[3 * BLK:]
            dvc_ref[:, gs] += dvg[3 * BLK:]
        dq_ref[...] = jnp.concatenate(dqs, axis=1)

    blk, ctx = _attn_specs(nb)
    out = pl.BlockSpec((BLK, ATW), lambda i: (i, 0))
    win = pl.BlockSpec((1, 3 * BLK, ATW), lambda i: (i, 0, 0))
    acc = _full((L, ATW))
    return _pcall(body, name="attn_bwd", grid=(nb,),
                  in_specs=[pl.BlockSpec(memory_space=pltpu.SMEM), blk(0),
                            blk(-1), blk(0), blk(1), ctx, blk(-1), blk(0), blk(1), ctx, out, out, out],
                  out_specs=[out, win, win, acc, acc, _full((8, 128))],
                  out_shape=[jax.ShapeDtypeStruct((s_len, ATW), F32),
                             jax.ShapeDtypeStruct((nb, 3 * BLK, ATW), F32),
                             jax.ShapeDtypeStruct((nb, 3 * BLK, ATW), F32),
                             jax.ShapeDtypeStruct((L, ATW), F32), jax.ShapeDtypeStruct((L, ATW), F32),
                             jax.ShapeDtypeStruct((8, 128), F32)], carry=carry)(
                      sinks, qr, k4, k4, k4, k4, v4, v4, v4, v4, y, lse, dy)


def _attn_post(p, cos, sin, qnw8, knw2, bd512, bd128, dupt, dq, dkw, dvw, dkc, dvc, dp, carry=None):
    tt = p.shape[0]
    s_len = tt - L
    nb = s_len // BLK
    nctx = L // BLK

    def body(q_ref, kv_ref, cos_ref, sin_ref, qw_ref, kw_ref, b5_ref, b1_ref, dupt_ref,
             dq_ref, kwp, kwo, kwn, vwp, vwo, vwn, dkc_ref, dvc_ref, _dp_in,
             dp_ref, dqw_ref, dkw_ref):
        t = pl.program_id(0)
        j = t - nctx

        @pl.when(t == 0)
        def _():
            dqw_ref[...] = jnp.zeros_like(dqw_ref)
            dkw_ref[...] = jnp.zeros_like(dkw_ref)

        is_lat = t >= nctx
        cos_, sin_ = cos_ref[...], sin_ref[...]
        has_p = is_lat & (j >= 1)
        has_n = is_lat & (j <= nb - 2)
        dk4 = (jnp.where(is_lat, kwo[0], dkc_ref[...]) + jnp.where(has_p, kwp[0], 0.0)
               + jnp.where(has_n, kwn[0], 0.0))
        dv4 = (jnp.where(is_lat, vwo[0], dvc_ref[...]) + jnp.where(has_p, vwp[0], 0.0)
               + jnp.where(has_n, vwn[0], 0.0))
        dkr = _dot(dk4, dupt_ref[...], prec=HI)
        dv = _dot(dv4, dupt_ref[...], prec=HI)
        kv = kv_ref[...]
        k = kv[:, :128]
        kw = kw_ref[...]
        rk = lax.rsqrt(_head_mean(k * k, b1_ref[...]) + EPS)
        xk = k * rk
        dkn = dkr * cos_ + _rot(dkr * sin_)
        dxk = dkn * kw
        dk = rk * (dxk - xk * _head_mean(dxk * xk, b1_ref[...]))
        dkw_ref[...] += jnp.sum(dkn * xk, axis=0, keepdims=True)
        q = q_ref[...]
        qw = qw_ref[...]
        rq = lax.rsqrt(_head_mean(q * q, b5_ref[...]) + EPS)
        xq = q * rq
        cos4 = jnp.concatenate([cos_] * 4, axis=1)
        sin4 = jnp.concatenate([sin_] * 4, axis=1)
        dqr = jnp.where(is_lat, dq_ref[...], 0.0) * (HDIM ** -0.5)
        dqn = dqr * cos4 + _rot(dqr * sin4)
        dxq = dqn * qw
        dqraw = rq * (dxq - xq * _head_mean(dxq * xq, b5_ref[...]))
        dqw_ref[...] += jnp.sum(dqn * xq, axis=0, keepdims=True)
        dp_ref[...] = jnp.concatenate([dqraw, dk, dv], axis=1).astype(BF16)

    row = lambda w, cb: pl.BlockSpec((BLK, w), lambda t: (t, cb))
    lat = pl.BlockSpec((BLK, ATW), lambda t: (jnp.maximum(t - nctx, 0), 0))

    def part(off):
        return pl.BlockSpec((1, BLK, ATW), lambda t: (jnp.clip(t - nctx + off, 0, nb - 1), 1 - off, 0))

    cacc = pl.BlockSpec((BLK, ATW), lambda t: (jnp.minimum(t, nctx - 1), 0))
    return _pcall(body, name="attn_post", grid=(tt // BLK,),
                  in_specs=[row(ATW, C_QRAW), row(256, C_KV), row(128, 0), row(128, 0),
                            _full((1, ATW)), _full((1, 128)), _full((ATW, ATW)), _full((128, 128)),
                            _full((ATW, 128)), lat, part(-1), part(0), part(1), part(-1), part(0), part(1),
                            cacc, cacc, ANY],
                  out_specs=[pl.BlockSpec((BLK, 768), lambda t: (t, C_QKV)), _full((1, ATW)), _full((1, 128))],
                  out_shape=[jax.ShapeDtypeStruct(dp.shape, BF16), jax.ShapeDtypeStruct((1, ATW), F32),
                             jax.ShapeDtypeStruct((1, 128), F32)],
                  aliases={18: 0}, carry=carry)(p, p, cos, sin, qnw8, knw2, bd512, bd128, dupt,
                                   dq, dkw, dkw, dkw, dvw, dvw, dvw, dkc, dvc, dp)


def _merge(ah, aa, p):
    s_len = ah.shape[0]

    def body(ah_ref, aa_ref, gh_ref, ga_ref, m_ref):
        m_ref[...] = (_sig(gh_ref[...]) * ah_ref[...] + _sig(ga_ref[...]) * aa_ref[...]).astype(BF16)

    row = pl.BlockSpec((TM, D), lambda i: (i, 0))
    return _pcall(body, name="merge", grid=(s_len // TM,),
                  in_specs=[row, row, pl.BlockSpec((TM, D), lambda i: (i + 1, 2)),
                            pl.BlockSpec((TM, D), lambda i: (i + 1, 3))],
                  out_specs=row, out_shape=jax.ShapeDtypeStruct((s_len, D), BF16))(ah, aa, p, p)


def _merge_bwd(dm, ah, aa, p, carry=None):
    tt = p.shape[0]
    s_len = tt - L

    def body(dm_ref, ah_ref, aa_ref, gh_ref, ga_ref, dp_ref, dmh_ref, dma_ref):
        i = pl.program_id(0)

        @pl.when(i == 0)
        def _():
            dp_ref[...] = jnp.zeros_like(dp_ref)

        @pl.when(i >= 1)
        def _():
            dm_ = dm_ref[...]
            sh, sa = _sig(gh_ref[...]), _sig(ga_ref[...])
            dp_ref[...] = jnp.concatenate([dm_ * ah_ref[...] * sh * (1.0 - sh),
                                           dm_ * aa_ref[...] * sa * (1.0 - sa)], axis=1).astype(BF16)
            dmh_ref[...] = (dm_ * sh).astype(BF16)
            dma_ref[...] = (dm_ * sa).astype(BF16)

    lat = pl.BlockSpec((TM, D), lambda i: (jnp.maximum(i - 1, 0), 0))
    return _pcall(body, name="merge_bwd", grid=(tt // TM,),
                  in_specs=[lat, lat, lat, pl.BlockSpec((TM, D), lambda i: (i, 2)),
                            pl.BlockSpec((TM, D), lambda i: (i, 3))],
                  out_specs=[pl.BlockSpec((TM, 2 * D), lambda i: (i, C_GATES)), lat, lat],
                  out_shape=[jax.ShapeDtypeStruct((tt, NCOL), BF16), jax.ShapeDtypeStruct((s_len, D), BF16),
                             jax.ShapeDtypeStruct((s_len, D), BF16)], carry=carry)(dm, ah, aa, p, p)


def _local_step(x, ctx, tgt, mod, modc, nw1, nw2, lg, hw, qnw, knw, sinks,
                w_in, wts, dist=None):
    s_len = x.shape[0]
    tt = s_len + L
    ss1 = jnp.stack([modc, mod[0:2]])
    ss2 = mod[3:5][None]
    g1, g2 = mod[2:3], mod[5:6]
    hw4 = jnp.tile(hw, (1, 4))
    qnw8 = jnp.tile(qnw, (1, 8))
    knw2 = jnp.tile(knw, (1, 2))
    cos, sin = _rope_tables(s_len)
    bd512, bd128 = _blockdiag(ATW, HDIM), _blockdiag(128, HDIM)
    dupm = _dup_matrix()
    dup, dupt = jnp.asarray(dupm, BF16), jnp.asarray(dupm.T, F32)
    tmt = tt

    def four(b):
        return b.reshape(4, 2 * b.shape[1], b.shape[2])

    def halves(g):
        return g.reshape(4, 2, g.shape[1] // 2, g.shape[2])

    h = _mod1(ctx, x, nw1, ss1)
    if dist is None:
        bh4, ba4, w_o, g4, u4, dn4 = wts
        p = _mm_in(h, w_in, tmt)
        o0, st0 = _hgrn_fwd(p, lg, rev=False)
        o1, st1 = _hgrn_fwd(p, lg, rev=True)
    else:
        core, chip = dist
        half = wts[3].shape[1] // 2
        p, first = _mm_in(h, w_in, tmt, carry=_carry_join(_carry_gather(list(wts[0:3])),
                                                          _carry_gather([wts[3]], rows=[(0, half)])))
        (o0, st0), (g8,) = _hgrn_fwd(p, lg, rev=False, carry=_carry_gather([first[3]], rows=[(half, half)]))
        o1, st1 = _hgrn_fwd(p, lg, rev=True)
        bh4, ba4, w_o, g4 = four(first[0]), four(first[1]), four(first[2]).reshape(D, D), four(g8)
    y_hg = _readout(o0, o1, p, hw4)
    qr, k4, v4 = _qk_prep(p, cos, sin, qnw8, knw2, bd512, bd128, dup)
    if dist is None:
        y_at, lse = _attn_fwd(qr, k4, v4, sinks)
    else:
        (y_at, lse), (u8,) = _attn_fwd(qr, k4, v4, sinks, carry=_carry_gather([wts[4]]))
        u4 = four(u8)
    ah = _mm_cs(y_hg, bh4, name="mm_bh")
    aa = _mm_cs(y_at, ba4, name="mm_ba")
    mixed = _merge(ah, aa, p)
    ao, x1, h2 = _out_proj_mod2(mixed, w_o, x, g1, nw2, ss2)
    if dist is None:
        a4, b4, z4 = _ffn_up(h2, g4, u4)
    else:
        (a4, b4, z4), (dn8,) = _ffn_up(h2, g4, u4, carry=_carry_gather([wts[5]]))
        dn4 = four(dn8)
    sq, dx2, dyb, dg2 = _ffn_down_loss(z4, dn4, x1, g2, tgt)

    da4, db4 = _ffn_dz(dyb, dn4, a4, b4)
    g_dn = _ffn_gdn(z4, dyb)
    if dist is None:
        dh2 = _ffn_dh2(da4, db4, g4, u4)
    else:
        dn_units = [halves(g_dn)]
        dh2, dn_recv = _ffn_dh2(da4, db4, g4, u4, carry=_carry_pairx(dn_units))
        dn_pairs = _rs_pair_add(dn_units, dn_recv, core)
    if dist is None:
        g_g, g_u = _ffn_ggu(h2, da4, db4)
    else:
        (g_g, g_u), c_dn = _ffn_ggu(h2, da4, db4, carry=_carry_chipx(dn_pairs))
        red_dn = _rs_chip_add(dn_pairs, c_dn, core, chip)
    dx1, dattn, dss2, dnw2, dg1 = _mod2_bwd(x1, dh2, dx2, ao, nw2, ss2, g1)
    dm = _mm(dattn, w_o, name="mm_dm", mode="nt", tm=512, tn=D, tk=D)
    g_o = _mm(mixed, dattn, name="mm_go", mode="tn", tm=D, tn=D, tk=512)
    if dist is None:
        dp, dmh, dma = _merge_bwd(dm, ah, aa, p)
    else:
        gu_units = [halves(g_g), halves(g_u)]
        (dp, dmh, dma), gu_recv = _merge_bwd(dm, ah, aa, p, carry=_carry_pairx(gu_units))
        ffn_pairs = list(dn_pairs) + list(_rs_pair_add(gu_units, gu_recv, core))
    dy_hg = _mm_cs_nt(dmh, bh4, name="mm_dyh")
    dy_at = _mm_cs_nt(dma, ba4, name="mm_dya")
    g_bh = _mm_cs_tn(y_hg, dmh, D // 4, name="mm_gbh")
    g_ba = _mm_cs_tn(y_at, dma, D // 4, name="mm_gba")
    if dist is None:
        dp, do, dhw4 = _readout_bwd(o0, o1, p, hw4, dy_hg, dp)
        dq, dkw, dvw, dkc, dvc, dsk = _attn_bwd(qr, k4, v4, sinks, y_at, lse, dy_at)
        dp, dqnw8, dknw2 = _attn_post(p, cos, sin, qnw8, knw2, bd512, bd128, dupt, dq, dkw, dvw, dkc, dvc, dp)
    else:
        mix_units = [halves(g_bh), halves(g_ba), halves(g_o.reshape(4, D // 4, D))]
        (dp, do, dhw4), mix_recv = _readout_bwd(o0, o1, p, hw4, dy_hg, dp, carry=_carry_pairx(mix_units))
        mix_pairs = _rs_pair_add(mix_units, mix_recv, core)
        (dq, dkw, dvw, dkc, dvc, dsk), bwd = _attn_bwd(
            qr, k4, v4, sinks, y_at, lse, dy_at,
            carry=_carry_join(_carry_chipx(ffn_pairs[1:2]), _carry_sibx(red_dn)))
        red_g = _rs_chip_add(ffn_pairs[1:2], bwd[0:1], core, chip)
        (dp, dqnw8, dknw2), post = _attn_post(
            p, cos, sin, qnw8, knw2, bd512, bd128, dupt, dq, dkw, dvw, dkc, dvc, dp,
            carry=_carry_join(_carry_chipx(ffn_pairs[2:3]), _carry_sibx(red_g)))
        red_u = _rs_chip_add(ffn_pairs[2:3], post[0:1], core, chip)
    if dist is None:
        dp, dv0, dq0, dlg0 = _hgrn_bwd(p, lg, do, st0, dp, None, rev=False)
        dp, dlg1 = _hgrn_bwd(p, lg, do, st1, dp, (dv0, dq0), rev=True)
    else:
        (dp, dv0, dq0, dlg0), mid = _hgrn_bwd(p, lg, do, st0, dp, None, rev=False,
                                              carry=_carry_join(_carry_chipx(mix_pairs), _carry_sibx(red_u)))
        mix_reds = _rs_chip_add(mix_pairs, mid[0:3], core, chip)
        (dp, dlg1), mix_done = _hgrn_bwd(p, lg, do, st1, dp, (dv0, dq0), rev=True, carry=_carry_sibx(mix_reds))
        ffn_done = bwd[1:2] + post[1:2] + mid[3:4]
    dh = _mm_dh(dp, w_in, tmt)
    g_in = _mm_gin(dp, h, tmt)
    if dist is None:
        gx, dss1, dnw1 = _mod1_bwd(ctx, x, dh, dx1, nw1, ss1)
        rs = None
    else:
        in_units = [halves(g_in.reshape(4, NCOL // 4, D))]
        (gx, dss1, dnw1), in_recv = _mod1_bwd(ctx, x, dh, dx1, nw1, ss1, carry=_carry_pairx(in_units))
        rs = dict(ffn_done=ffn_done, mix_done=mix_done, in_units=in_units, in_recv=in_recv)

    dmod = jnp.concatenate([dss1[1], dg1, dss2, dg2], axis=0)
    dmodc = dss1[0]
    raw = (dss1, dg1, dss2, dg2, dnw1, dnw2, dhw4, dqnw8, dknw2, dsk, dlg0, dlg1)
    small = dict(raw=raw, dmod=dmod, dmodc=dmodc, dnw1=dnw1, dnw2=dnw2,
                 dhw=dhw4.reshape(4, HGD).sum(0, keepdims=True),
                 dqnw=dqnw8.reshape(8, HDIM).sum(0, keepdims=True),
                 dknw=dknw2.reshape(2, HDIM).sum(0, keepdims=True),
                 dsinks=dsk[:, 0], dlg=jnp.concatenate([dlg0, dlg1], axis=0))
    big = dict(w_in=g_in, w_bh=g_bh, w_ba=g_ba, w_o=g_o, w_g=g_g, w_u=g_u, w_dn=g_dn)
    return sq, gx, big, small, rs


def _place():
    x, y, c = lax.axis_index("x"), lax.axis_index("y"), lax.axis_index("c")
    return x, y, c


def _gather_blocks(x_refs, out_refs, send_sems, recv_sems, local_sems):
    n = len(out_refs)
    x, y, c = _place()
    me, sibling = (x, y, c), (x, y, 1 - c)
    chips = [(1 - x, y), (x, 1 - y), (1 - x, 1 - y)]

    def slot(u, px, py, pc):
        return out_refs[u].at[4 * px + 2 * py + pc]

    def copy(u, k, block, to, src=None):
        return pltpu.make_async_remote_copy(
            src_ref=slot(u, *block) if src is None else src, dst_ref=slot(u, *block),
            send_sem=send_sems.at[u, k], recv_sem=recv_sems.at[u, k], device_id=to, device_id_type=MESH)

    mines = []
    if x_refs is not None:
        mines = [pltpu.make_async_copy(x_refs[u], slot(u, *me), local_sems.at[u]) for u in range(n)]
    for cp in mines:
        cp.start()
    first = []
    for u in range(n):
        src = None if x_refs is None else x_refs[u]
        first.append(copy(u, 0, me, sibling, src=src))
        first += [copy(u, 1 + j, me, (*chip, c), src=src) for j, chip in enumerate(chips)]
    for cp in first:
        cp.start()
    passed = []
    for j, chip in enumerate(chips):
        for u in range(n):
            copy(u, 1 + j, (*chip, c), me).wait_recv()
            fwd = copy(u, 4 + j, (*chip, c), sibling)
            fwd.start()
            passed.append(fwd)
    for u in range(n):
        copy(u, 0, sibling, me).wait_recv()
    for j, chip in enumerate(chips):
        for u in range(n):
            copy(u, 4 + j, (*chip, 1 - c), me).wait_recv()
    for cp in first + passed:
        cp.wait_send()
    for cp in mines:
        cp.wait()


def _gather_sems(n):
    return [pltpu.SemaphoreType.DMA((n, 7)), pltpu.SemaphoreType.DMA((n, 7)), pltpu.SemaphoreType.DMA((n,))]


def _allgather(blks, *, name, in_vmem):
    n = len(blks)
    space = pltpu.VMEM if in_vmem else pl.ANY

    def body(*refs):
        _gather_blocks(refs[:n], refs[n:2 * n], *refs[2 * n:])

    return pl.pallas_call(
        body, name=name, out_shape=[jax.ShapeDtypeStruct((8,) + b.shape, b.dtype) for b in blks],
        in_specs=[pl.BlockSpec(memory_space=space)] * n, out_specs=[pl.BlockSpec(memory_space=space)] * n,
        scratch_shapes=_gather_sems(n))(*blks)


def _cast_place(ws, c, dev):
    n = len(ws)

    def body(s_ref, *refs):
        for u in range(n):
            refs[n + u][0] = refs[u][...].astype(BF16)

    in_specs, out_specs, out_shape = [], [], []
    for w in ws:
        q, cols = w.shape[0] // 4, w.shape[1]
        in_specs.append(pl.BlockSpec((q, cols), lambda i, s: (2 * s[0] + i, 0)))
        out_specs.append(pl.BlockSpec((1, q, cols), lambda i, s: (s[1], i, 0)))
        out_shape.append(jax.ShapeDtypeStruct((8, 2 * q, cols), BF16))
    return pl.pallas_call(
        body, name="cast_place",
        grid_spec=pltpu.PrefetchScalarGridSpec(num_scalar_prefetch=1, grid=(2,), in_specs=in_specs,
                                               out_specs=out_specs),
        out_shape=_out_hbm(out_shape),
        compiler_params=pltpu.CompilerParams(vmem_limit_bytes=48 << 20))(jnp.stack([c, dev]), *_in_hbm(ws))


def _gather_phases(out_refs, send_sems, recv_sems, rows=None):
    n = len(out_refs)
    x, y, c = _place()
    me, sibling = (x, y, c), (x, y, 1 - c)
    chips = [(1 - x, y), (x, 1 - y), (1 - x, 1 - y)]

    def copy(u, k, block, to):
        px, py, pc = block
        ref = out_refs[u].at[4 * px + 2 * py + pc]
        if rows is not None and rows[u] is not None:
            ref = ref.at[pl.ds(rows[u][0], rows[u][1])]
        return pltpu.make_async_remote_copy(src_ref=ref, dst_ref=ref, send_sem=send_sems.at[u, k],
                                            recv_sem=recv_sems.at[u, k], device_id=to, device_id_type=MESH)

    def start():
        for u in range(n):
            copy(u, 0, me, sibling).start()
            for j, chip in enumerate(chips):
                copy(u, 1 + j, me, (*chip, c)).start()

    def mid():
        for j, chip in enumerate(chips):
            for u in range(n):
                copy(u, 1 + j, (*chip, c), me).wait_recv()
                copy(u, 4 + j, (*chip, c), sibling).start()

    def end():
        for u in range(n):
            copy(u, 0, sibling, me).wait_recv()
        for j, chip in enumerate(chips):
            for u in range(n):
                copy(u, 4 + j, (*chip, 1 - c), me).wait_recv()
        for u in range(n):
            copy(u, 0, me, sibling).wait_send()
            for j, chip in enumerate(chips):
                copy(u, 1 + j, me, (*chip, c)).wait_send()
                copy(u, 4 + j, (*chip, c), sibling).wait_send()

    return start, mid, end


def _carry_gather(bufs, rows=None):
    n = len(bufs)
    return _Carry(bufs, [jax.ShapeDtypeStruct(b.shape, b.dtype) for b in bufs], {u: u for u in range(n)},
                  [pltpu.SemaphoreType.DMA((n, 7)), pltpu.SemaphoreType.DMA((n, 7))],
                  lambda ins, outs, sems: _gather_phases(outs, *sems, rows=rows))


def _allgather_inplace(bufs, *, name):
    n = len(bufs)

    def body(*refs):
        for phase in _gather_phases(refs[n:2 * n], *refs[2 * n:]):
            phase()

    return pl.pallas_call(
        body, name=name, out_shape=[jax.ShapeDtypeStruct(b.shape, b.dtype) for b in bufs],
        in_specs=[ANY] * n, out_specs=[ANY] * n, input_output_aliases={u: u for u in range(n)},
        scratch_shapes=[pltpu.SemaphoreType.DMA((n, 7)), pltpu.SemaphoreType.DMA((n, 7))])(*bufs)


def _ag_small(raw):
    def body(dss1, dg1, dss2, dg2, dnw1, dnw2, dhw4, dqnw8, dknw2, dsk, dlg0, dlg1,
             out_ref, tot_ref, blk, send_sems, recv_sems, local_sems):
        blk[...] = jnp.zeros_like(blk)
        blk[0:2, :] = dss1[1]
        blk[2:3, :] = dg1[...]
        blk[3:5, :] = dss2[...]
        blk[5:6, :] = dg2[...]
        blk[6:8, :] = dss1[0]
        blk[8:9, :] = dnw1[...]
        blk[9:10, :] = dnw2[...]
        blk[10:11, 0:HGW] = dhw4[...]
        blk[10:11, HGW:D] = dqnw8[...]
        blk[11:12, 0:128] = dknw2[...]
        blk[12:14, 0:HGW] = dlg0[0]
        blk[14:16, 0:HGW] = dlg1[0]
        blk[16:24, 0:128] = dsk[...]
        _gather_blocks([blk], [out_ref], send_sems, recv_sems, local_sems)
        acc = out_ref[0]
        for i in range(1, 8):
            acc = acc + out_ref[i]
        tot_ref[...] = acc

    vm = pl.BlockSpec(memory_space=pltpu.VMEM)
    return pl.pallas_call(
        body, name="ag_small",
        out_shape=[jax.ShapeDtypeStruct((8, 24, D), F32), jax.ShapeDtypeStruct((24, D), F32)],
        in_specs=[vm] * 12, out_specs=[vm, vm],
        scratch_shapes=[pltpu.VMEM((24, D), F32)] + _gather_sems(1))(*raw)


def _rs_pair_exchange(units):
    n = len(units)

    def body(*refs):
        start, _, end = _pairx_phases(refs[:n], refs[n:2 * n], *refs[2 * n:])
        start()
        end()

    return pl.pallas_call(
        body, name="rs_pair_exchange", out_shape=_pairx_shapes(units),
        in_specs=[ANY] * n, out_specs=[ANY] * n,
        scratch_shapes=[pltpu.SemaphoreType.DMA((n, 4)), pltpu.SemaphoreType.DMA((n, 4))])(*units)


def _pairx_shapes(units):
    return [jax.ShapeDtypeStruct((4,) + g.shape[2:], g.dtype) for g in units]


def _pairx_phases(g_refs, r_refs, send_sems, recv_sems):
    n = len(g_refs)
    x, y, c = _place()
    cps = [pltpu.make_async_remote_copy(
        src_ref=g_refs[u].at[j, 1 - c], dst_ref=r_refs[u].at[j], send_sem=send_sems.at[u, j],
        recv_sem=recv_sems.at[u, j], device_id=(x, y, 1 - c), device_id_type=MESH)
        for u in range(n) for j in range(4)]

    def start():
        for cp in cps:
            cp.start()

    def end():
        for cp in cps:
            cp.wait()

    return start, None, end


def _carry_pairx(units):
    n = len(units)
    return _Carry(units, _pairx_shapes(units), {},
                  [pltpu.SemaphoreType.DMA((n, 4)), pltpu.SemaphoreType.DMA((n, 4))],
                  lambda ins, outs, sems: _pairx_phases(ins, outs, *sems))


def _rs_pair_add(units, recvs, c):
    n = len(units)

    def body(c_ref, *refs):
        for u in range(n):
            refs[2 * n + u][...] = (refs[u][0] + refs[n + u][...]).astype(BF16)

    in_specs, out_specs, out_shape = [], [], []
    for g in units:
        h, w = g.shape[2] // 2, g.shape[3]
        in_specs.append(pl.BlockSpec((1, 1, h, w), lambda j, i, cr: (j, cr[0], i, 0)))
    for g in units:
        h, w = g.shape[2] // 2, g.shape[3]
        in_specs.append(pl.BlockSpec((1, h, w), lambda j, i, cr: (j, i, 0)))
        out_specs.append(pl.BlockSpec((1, h, w), lambda j, i, cr: (j, i, 0)))
        out_shape.append(jax.ShapeDtypeStruct((4, 2 * h, w), BF16))
    return pl.pallas_call(
        body, name="rs_pair_add",
        grid_spec=pltpu.PrefetchScalarGridSpec(num_scalar_prefetch=1, grid=(4, 2), in_specs=in_specs,
                                               out_specs=out_specs),
        out_shape=_out_hbm(out_shape),
        compiler_params=pltpu.CompilerParams(vmem_limit_bytes=48 << 20))(
            c.reshape(1), *_in_hbm(list(units) + list(recvs)))


def _rs_chip_exchange(pairs):
    n = len(pairs)

    def body(*refs):
        start, _, end = _chipx_phases(refs[:n], refs[n:2 * n], *refs[2 * n:])
        start()
        end()

    return pl.pallas_call(
        body, name="rs_chip_exchange", out_shape=[jax.ShapeDtypeStruct(p.shape, p.dtype) for p in pairs],
        in_specs=[ANY] * n, out_specs=[ANY] * n,
        scratch_shapes=[pltpu.SemaphoreType.DMA((n, 3)), pltpu.SemaphoreType.DMA((n, 3))])(*pairs)


def _chipx_phases(p_refs, r_refs, send_sems, recv_sems):
    n = len(p_refs)
    x, y, c = _place()
    k = 2 * x + y
    sends = []
    for d in range(1, 4):
        j = (k + d) % 4
        for u in range(n):
            sends.append(pltpu.make_async_remote_copy(
                src_ref=p_refs[u].at[j], dst_ref=r_refs[u].at[k], send_sem=send_sems.at[u, d - 1],
                recv_sem=recv_sems.at[u, d - 1], device_id=(j // 2, j % 2, c), device_id_type=MESH))

    def start():
        for cp in sends:
            cp.start()

    def end():
        for d in range(1, 4):
            src = (k + 4 - d) % 4
            for u in range(n):
                pltpu.make_async_remote_copy(
                    src_ref=p_refs[u].at[src], dst_ref=r_refs[u].at[src], send_sem=send_sems.at[u, d - 1],
                    recv_sem=recv_sems.at[u, d - 1], device_id=(x, y, c), device_id_type=MESH).wait_recv()
        for cp in sends:
            cp.wait_send()

    return start, None, end


def _carry_chipx(pairs):
    n = len(pairs)
    return _Carry(pairs, [jax.ShapeDtypeStruct(p.shape, p.dtype) for p in pairs], {},
                  [pltpu.SemaphoreType.DMA((n, 3)), pltpu.SemaphoreType.DMA((n, 3))],
                  lambda ins, outs, sems: _chipx_phases(ins, outs, *sems))


def _rs_chip_add(pairs, contribs, c, chip):
    n = len(pairs)

    def body(s_ref, *refs):
        for u in range(n):
            a, b, c_, d = refs[4 * u:4 * u + 4]
            refs[4 * n + u][0] = ((a[0].astype(F32) + b[0].astype(F32)) + c_[0].astype(F32)) + d[0].astype(F32)

    in_specs, out_specs, out_shape, args = [], [], [], []
    for p, r in zip(pairs, contribs):
        h, w = p.shape[1] // 2, p.shape[2]
        in_specs += [pl.BlockSpec((1, h, w), functools.partial(lambda d, i, s: ((s[1] + d) % 4, i, 0), d))
                     for d in range(4)]
        args += [p, r, r, r]
        out_specs.append(pl.BlockSpec((1, h, w), lambda i, s: (s[0], i, 0)))
        out_shape.append(jax.ShapeDtypeStruct((2, 2 * h, w), F32))
    return pl.pallas_call(
        body, name="rs_chip_add",
        grid_spec=pltpu.PrefetchScalarGridSpec(num_scalar_prefetch=1, grid=(2,), in_specs=in_specs,
                                               out_specs=out_specs),
        out_shape=_out_hbm(out_shape),
        compiler_params=pltpu.CompilerParams(vmem_limit_bytes=48 << 20))(jnp.stack([c, chip]), *_in_hbm(args))


def _rs_sibling_gather(reds):
    n = len(reds)

    def body(*refs):
        start, _, end = _sibx_phases(refs[n:2 * n], *refs[2 * n:])
        start()
        end()

    return pl.pallas_call(
        body, name="rs_sibling_gather", out_shape=[jax.ShapeDtypeStruct(r.shape, r.dtype) for r in reds],
        in_specs=[ANY] * n, out_specs=[ANY] * n, input_output_aliases={u: u for u in range(n)},
        scratch_shapes=[pltpu.SemaphoreType.DMA((n,))] * 2)(*reds)


def _sibx_phases(o_refs, send_sems, recv_sems):
    n = len(o_refs)
    x, y, c = _place()
    cps = [pltpu.make_async_remote_copy(
        src_ref=o_refs[u].at[c], dst_ref=o_refs[u].at[c], send_sem=send_sems.at[u], recv_sem=recv_sems.at[u],
        device_id=(x, y, 1 - c), device_id_type=MESH) for u in range(n)]

    def start():
        for cp in cps:
            cp.start()

    def end():
        for u in range(n):
            cps[u].wait_send()
            pltpu.make_async_remote_copy(
                src_ref=o_refs[u].at[1 - c], dst_ref=o_refs[u].at[1 - c], send_sem=send_sems.at[u],
                recv_sem=recv_sems.at[u], device_id=(x, y, 1 - c), device_id_type=MESH).wait_recv()

    return start, None, end


def _carry_sibx(reds):
    n = len(reds)
    return _Carry(reds, [jax.ShapeDtypeStruct(r.shape, r.dtype) for r in reds], {u: u for u in range(n)},
                  [pltpu.SemaphoreType.DMA((n,))] * 2, lambda ins, outs, sems: _sibx_phases(outs, *sems))


def _prologue(blk, c_ctx, w, b, in8):
    n = w.shape[1]

    def body(blk_ref, cctx_ref, w_ref, b_ref, _in_in, g0_ref, c16_ref, g1_ref, in_ref, mod_s,
             s1, r1, l1, s2, r2, l2, s3, r3):
        start, mid, end = _gather_phases([in_ref], s3, r3)
        start()
        _gather_blocks([blk_ref], [g0_ref], s1, r1, l1)
        c16 = jnp.concatenate([g0_ref[i, 0:1, :] for i in range(8)] + [cctx_ref[...], jnp.zeros((7, D), F32)],
                              axis=0)
        c16_ref[...] = c16
        mod_s[...] = _dot(c16 * _sig(c16), w_ref[...], prec=HI) + b_ref[...]
        _gather_blocks([mod_s], [g1_ref], s2, r2, l2)
        mid()
        end()

    vm = pl.BlockSpec(memory_space=pltpu.VMEM)
    return pl.pallas_call(
        body, name="prologue",
        out_shape=[jax.ShapeDtypeStruct((8, 8, D), F32), jax.ShapeDtypeStruct((16, D), F32),
                   jax.ShapeDtypeStruct((8, 16, n), F32), jax.ShapeDtypeStruct(in8.shape, in8.dtype)],
        in_specs=[vm, vm, vm, vm, ANY], out_specs=[vm, vm, vm, ANY], input_output_aliases={4: 3},
        scratch_shapes=[pltpu.VMEM((16, n), F32)] + _gather_sems(1) + _gather_sems(1)
        + [pltpu.SemaphoreType.DMA((1, 7)), pltpu.SemaphoreType.DMA((1, 7))],
        compiler_params=pltpu.CompilerParams(vmem_limit_bytes=48 << 20))(blk, c_ctx, w, b, in8)


def _ada_bwd(c16, dmod16, w):
    n = w.shape[1]
    tn = 512

    def body(c_ref, d_ref, w_ref, gw_ref, gc_ref):
        j = pl.program_id(0)

        @pl.when(j == 0)
        def _():
            gc_ref[...] = jnp.zeros_like(gc_ref)

        cc = c_ref[...]
        dm = d_ref[...]
        gw_ref[...] = _dot(cc * _sig(cc), dm, TN, prec=HI)
        gc_ref[...] += _dot(dm, w_ref[...], NT, prec=HI)

    return _pcall(body, name="ada_bwd", grid=(n // tn,),
                  in_specs=[_full((16, D)), pl.BlockSpec((16, tn), lambda j: (0, j)),
                            pl.BlockSpec((D, tn), lambda j: (0, j))],
                  out_specs=[pl.BlockSpec((D, tn), lambda j: (0, j)), _full((16, D))],
                  out_shape=[jax.ShapeDtypeStruct((D, n), F32),
                             jax.ShapeDtypeStruct((16, D), F32)])(c16, dmod16, w)


def _adam_math(w, g, m, v):
    c1 = 1.0 - ADAM_B1 ** ADAM_STEP
    c2 = 1.0 - ADAM_B2 ** ADAM_STEP
    nm = ADAM_B1 * m + (1.0 - ADAM_B1) * g
    nv = ADAM_B2 * v + (1.0 - ADAM_B2) * (g * g)
    return -ADAM_LR * ((nm / c1) / (jnp.sqrt(nv / c2) + ADAM_EPS) + ADAM_WD * w), nm, nv


def _adamw_small(ws, gs, ms, vs):
    n = len(ws)

    def body(*refs):
        for u in range(n):
            d_, nm, nv = _adam_math(refs[u][...], refs[n + u][...], refs[2 * n + u][...], refs[3 * n + u][...])
            refs[4 * n + u][...] = d_
            refs[5 * n + u][...] = nm
            refs[6 * n + u][...] = nv

    specs = [_full(w.shape) for w in ws]
    shapes = [jax.ShapeDtypeStruct(w.shape, F32) for w in ws]
    out = _pcall(body, name="adamw_small", grid=(1,), in_specs=specs * 4, out_specs=specs * 3,
                 out_shape=shapes * 3)(*ws, *gs, *ms, *vs)
    return out[:n], out[n:2 * n], out[2 * n:]


def _cctx_grad(parts, c_ctx):
    def body(p_ref, c_ref, o_ref):
        acc = p_ref[0:1, :]
        for k in range(1, 4):
            acc = acc + p_ref[k:k + 1, :]
        cc = c_ref[...]
        s = _sig(cc)
        o_ref[...] = acc * (s * (1.0 + cc * (1.0 - s)))

    return _pcall(body, name="cctx_grad", grid=(1,), in_specs=[_full(parts.shape), _full((1, D))],
                  out_specs=_full((1, D)), out_shape=jax.ShapeDtypeStruct((1, D), F32))(parts, c_ctx)


ADAM_STEPS = 8


def _adamw_multi(ws, gs, ms, vs, *, name, carry=None):
    n = len(ws)

    def body(*refs):
        for u in range(n):
            refs[4 * n + u][...], refs[5 * n + u][...], refs[6 * n + u][...] = _adam_math(
                refs[u][...], refs[n + u][...], refs[2 * n + u][...], refs[3 * n + u][...])

    specs = [pl.BlockSpec((w.shape[0] // ADAM_STEPS, w.shape[1]), lambda i: (i, 0)) for w in ws]
    shapes = [jax.ShapeDtypeStruct(w.shape, F32) for w in ws]
    res = _pcall(body, name=name, grid=(ADAM_STEPS,), in_specs=specs * 4, out_specs=specs * 3,
                 out_shape=shapes * 3, carry=carry)(*ws, *gs, *ms, *vs)
    out, extra = res if carry is not None else (res, None)
    return (out[:n], out[n:2 * n], out[2 * n:]), extra


def kernel(x, c, ctx, c_ctx, w_ada, b_ada, norm_mix_w, norm_ffn_w, w_in, hgrn_lb_logits, hgrn_norm_w, q_norm_w, k_norm_w, attn_sinks, w_branch_hgrn, w_branch_attn, w_out, w_ffn_gate, w_ffn_up, w_ffn_down, loss_target, m_c_ctx, m_w_ada, m_b_ada, m_norm_mix_w, m_norm_ffn_w, m_w_in, m_hgrn_lb_logits, m_hgrn_norm_w, m_q_norm_w, m_k_norm_w, m_attn_sinks, m_w_branch_hgrn, m_w_branch_attn, m_w_out, m_w_ffn_gate, m_w_ffn_up, m_w_ffn_down, v_c_ctx, v_w_ada, v_b_ada, v_norm_mix_w, v_norm_ffn_w, v_w_in, v_hgrn_lb_logits, v_hgrn_norm_w, v_q_norm_w, v_k_norm_w, v_attn_sinks, v_w_branch_hgrn, v_w_branch_attn, v_w_out, v_w_ffn_gate, v_w_ffn_up, v_w_ffn_down):
    xi, yi, ci = _place()
    chip = 2 * xi + yi
    dev = 2 * chip + ci
    s_len = x.shape[1]

    shards = [w_in[0].T, w_branch_hgrn[0], w_branch_attn[0], w_out[0], w_ffn_gate[0].T, w_ffn_up[0].T,
              w_ffn_down[0]]
    bufs = _cast_place(shards, ci, dev)

    lbrow = jnp.pad(hgrn_lb_logits.reshape(1, 512), ((0, 0), (0, D - 512)))
    blk = jnp.concatenate([c, lbrow, jnp.zeros((6, D), F32)], axis=0)
    nada = w_ada.shape[2]
    b_sh = lax.dynamic_slice(b_ada, (0, chip * nada), (1, nada))
    g0, c16, g1, in8 = _prologue(blk, c_ctx[None], w_ada[0], b_sh, bufs[0])
    lg = g0[0::2, 1, :512].reshape(4, 2, 2, 128).transpose(1, 2, 0, 3).reshape(2, 2, HGW)
    modall = g1[0::2].transpose(1, 0, 2).reshape(16, 4 * nada)
    mod = lax.dynamic_slice(modall, (dev, 0), (1, 6 * D)).reshape(6, D)
    modc = modall[8].reshape(6, D)[:2]

    sq, gx, _, small, rs = _local_step(
        x[0], ctx[0], loss_target[0], mod, modc, norm_mix_w, norm_ffn_w, lg, hgrn_norm_w, q_norm_w,
        k_norm_w, attn_sinks[0], in8.reshape(NCOL, D), bufs[1:], dist=(ci, chip))
    loss = lax.psum(0.5 * jnp.sum(sq) / D, ("x", "y", "c"))

    def whole(r):
        return r.reshape(2 * r.shape[1], r.shape[2])

    g_dn, g_g, g_u = [whole(r) for r in rs["ffn_done"]]
    g_bh, g_ba, g_o = [whole(r) for r in rs["mix_done"]]
    in_pairs = _rs_pair_add(rs["in_units"], rs["in_recv"], ci)

    g2, tot = _ag_small(small["raw"])
    dmodc_tot = jnp.pad(tot[6:8].reshape(1, 2 * D), ((0, 0), (0, 4 * D)))
    g_b_ada = tot[0:6].reshape(1, 6 * D) + dmodc_tot
    dmod16 = jnp.concatenate([g2[:, 0:6].reshape(8, 6 * D), dmodc_tot, jnp.zeros((7, 6 * D), F32)], axis=0)
    g_w_ada, gc_part = _ada_bwd(c16, lax.dynamic_slice(dmod16, (0, chip * nada), (16, nada)), w_ada[0])
    g3, = _allgather([gc_part[8:16]], name="ag_cctx", in_vmem=True)
    g_c_ctx = _cctx_grad(g3[0::2, 0], c_ctx[None])[0]
    g_nw1 = tot[8:9]
    g_nw2 = tot[9:10]
    g_hw = tot[10, :HGW].reshape(4, HGD).sum(0, keepdims=True)
    g_qnw = tot[10, HGW:].reshape(8, HDIM).sum(0, keepdims=True)
    g_knw = tot[11, :128].reshape(2, HDIM).sum(0, keepdims=True)
    g_sinks = tot[16:24, 0][None]
    g_lg = lax.dynamic_slice(tot[12:16, :HGW].reshape(2, 2, HGW), (0, 0, chip * 128), (2, 2, 128))

    names = ["c_ctx", "w_ada", "b_ada", "norm_mix_w", "norm_ffn_w", "w_in", "hgrn_lb_logits", "hgrn_norm_w",
             "q_norm_w", "k_norm_w", "attn_sinks", "w_branch_hgrn", "w_branch_attn", "w_out", "w_ffn_gate",
             "w_ffn_up", "w_ffn_down"]
    ws = dict(zip(names, [c_ctx, w_ada, b_ada, norm_mix_w, norm_ffn_w, w_in, hgrn_lb_logits, hgrn_norm_w,
                          q_norm_w, k_norm_w, attn_sinks, w_branch_hgrn, w_branch_attn, w_out, w_ffn_gate,
                          w_ffn_up, w_ffn_down]))
    ms = dict(zip(names, [m_c_ctx, m_w_ada, m_b_ada, m_norm_mix_w, m_norm_ffn_w, m_w_in, m_hgrn_lb_logits,
                          m_hgrn_norm_w, m_q_norm_w, m_k_norm_w, m_attn_sinks, m_w_branch_hgrn,
                          m_w_branch_attn, m_w_out, m_w_ffn_gate, m_w_ffn_up, m_w_ffn_down]))
    vs = dict(zip(names, [v_c_ctx, v_w_ada, v_b_ada, v_norm_mix_w, v_norm_ffn_w, v_w_in, v_hgrn_lb_logits,
                          v_hgrn_norm_w, v_q_norm_w, v_k_norm_w, v_attn_sinks, v_w_branch_hgrn,
                          v_w_branch_attn, v_w_out, v_w_ffn_gate, v_w_ffn_up, v_w_ffn_down]))
    transposed = ("w_in", "w_ffn_gate", "w_ffn_up")

    def view(a, n):
        return a[0].T if n in transposed else a[0]

    def unview(a, n):
        return a.T[None] if n in transposed else a[None]

    delta, new_m, new_v, grads = {}, {}, {}, {}

    def big_adamw(group, gs, name, carry=None):
        (d_, m_, v_), extra = _adamw_multi([view(ws[n], n) for n in group], gs, [view(ms[n], n) for n in group],
                                           [view(vs[n], n) for n in group], name=name, carry=carry)
        for i, n in enumerate(group):
            grads[n], delta[n], new_m[n], new_v[n] = (unview(gs[i], n), unview(d_[i], n), unview(m_[i], n),
                                                      unview(v_[i], n))
        return extra

    in_contribs = big_adamw(["w_ffn_down", "w_ffn_gate", "w_ffn_up", "w_out", "w_branch_hgrn", "w_branch_attn"],
                            [g_dn, g_g, g_u, g_o, g_bh, g_ba], "adamw_first", carry=_carry_chipx(in_pairs))
    in_reds = _rs_chip_add(in_pairs, in_contribs, ci, chip)
    g_in, = [whole(r) for r in _rs_sibling_gather(in_reds)]
    big_adamw(["w_in", "w_ada"], [g_in, g_w_ada], "adamw_second")
    grads.update(c_ctx=g_c_ctx, b_ada=g_b_ada, norm_mix_w=g_nw1, norm_ffn_w=g_nw2, hgrn_lb_logits=g_lg,
                 hgrn_norm_w=g_hw, q_norm_w=g_qnw, k_norm_w=g_knw, attn_sinks=g_sinks)
    small_names = [n for n in names if n not in delta]

    def two_d(a):
        return a.reshape(1, -1) if a.ndim == 1 else a

    sd, sm_, sv = _adamw_small(*[[two_d(d[n]) for n in small_names] for d in (ws, grads, ms, vs)])
    for i, n in enumerate(small_names):
        for dst, src in ((delta, sd), (new_m, sm_), (new_v, sv)):
            dst[n] = src[i].reshape(ws[n].shape)
    return (loss, gx[None], *[grads[n] for n in names], *[delta[n] for n in names],
            *[new_m[n] for n in names], *[new_v[n] for n in names])
```

```python
import functools

import numpy as np
import jax
import jax.numpy as jnp
from jax import lax
from jax.experimental import pallas as pl
from jax.experimental.pallas import tpu as pltpu

F32 = jnp.float32
BF16 = jnp.bfloat16
HI = lax.Precision.HIGHEST
MESH = pl.DeviceIdType.MESH

D = 1024
L = 256
TM = 256
HGW = 512
HGD = 128
CH = 32
ATW = 512
HDIM = 64
BLK = 128
GRID_W = 64
DFF = 2816
NCOL = 5376
EPS = 1e-6
ROPE_THETA = 10000.0

C_FB, C_INP, C_QHG, C_FF = 0, 1, 2, 3
C_GATES = 1
C_GHG, C_QRAW = 8, 9
C_KV = 20
C_QKV = 6

ADAM_LR, ADAM_B1, ADAM_B2, ADAM_EPS, ADAM_WD, ADAM_STEP = 0.001, 0.9, 0.999, 1e-08, 0.01, 10

NN = (((1,), (0,)), ((), ()))
NT = (((1,), (1,)), ((), ()))
TN = (((0,), (0,)), ((), ()))


def _dot(a, b, dims=NN, prec=None):
    return lax.dot_general(a, b, dims, precision=prec, preferred_element_type=F32)


def _bdot(a, b, dims=NN):
    return _dot(a.astype(BF16), b.astype(BF16), dims)


def _sig(x):
    return 1.0 / (1.0 + jnp.exp(-x))


class _Carry:
    def __init__(self, ins, outs, aliases, scratch, phases):
        self.ins, self.outs, self.aliases, self.scratch, self.phases = ins, outs, aliases, scratch, phases


def _in_hbm(args):
    return [pltpu.with_memory_space_constraint(a, pltpu.HBM) for a in args]


def _out_hbm(shapes):
    if isinstance(shapes, (list, tuple)):
        return [pltpu.HBM(s.shape, s.dtype) for s in shapes]
    return pltpu.HBM(shapes.shape, shapes.dtype)


def _carry_join(a, b):
    na_in, na_out, na_sc = len(a.ins), len(a.outs), len(a.scratch)
    aliases = dict(a.aliases)
    aliases.update({na_in + i: na_out + o for i, o in b.aliases.items()})

    def phases(ins, outs, sems):
        pa = a.phases(ins[:na_in], outs[:na_out], sems[:na_sc])
        pb = b.phases(ins[na_in:], outs[na_out:], sems[na_sc:])

        def both(fa, fb):
            if fa is None and fb is None:
                return None

            def run():
                for fn in (fa, fb):
                    if fn is not None:
                        fn()
            return run

        return tuple(both(fa, fb) for fa, fb in zip(pa, pb))

    return _Carry(list(a.ins) + list(b.ins), list(a.outs) + list(b.outs), aliases,
                  list(a.scratch) + list(b.scratch), phases)


def _pcall(body, *, name, grid, in_specs, out_specs, out_shape, scratch=(), aliases=None, vmem_mb=48,
           carry=None):
    params = pltpu.CompilerParams(dimension_semantics=("arbitrary",) * len(grid),
                                  vmem_limit_bytes=vmem_mb << 20)
    if carry is None:
        plain = pl.pallas_call(
            body, name=name, grid=grid, in_specs=in_specs, out_specs=out_specs, out_shape=_out_hbm(out_shape),
            scratch_shapes=list(scratch), input_output_aliases=aliases or {}, compiler_params=params)
        return lambda *args: plain(*_in_hbm(args))
    single = not isinstance(out_shape, (list, tuple))
    out_specs_l = [out_specs] if single else list(out_specs)
    out_shape_l = [out_shape] if single else list(out_shape)
    n_in, n_out, n_sc = len(in_specs), len(out_shape_l), len(scratch)
    k_in, k_out = len(carry.ins), len(carry.outs)
    nsteps = int(np.prod(grid))
    assert nsteps >= 3

    def wrapped(*refs):
        ins, cins = refs[:n_in], refs[n_in:n_in + k_in]
        o0 = n_in + k_in
        outs, couts = refs[o0:o0 + n_out], refs[o0 + n_out:o0 + n_out + k_out]
        s0 = o0 + n_out + k_out
        sc, csc = refs[s0:s0 + n_sc], refs[s0 + n_sc:]
        step = pl.program_id(0)
        for ax in range(1, len(grid)):
            step = step * grid[ax] + pl.program_id(ax)
        start, mid, end = carry.phases(cins, couts, csc)
        pl.when(step == 0)(start)
        body(*ins, *outs, *sc)
        if mid is not None:
            pl.when(step == nsteps - 2)(mid)
        pl.when(step == nsteps - 1)(end)

    all_aliases = dict(aliases or {})
    all_aliases.update({n_in + i: n_out + o for i, o in carry.aliases.items()})
    call = pl.pallas_call(
        wrapped, name=name, grid=grid, in_specs=list(in_specs) + [ANY] * k_in,
        out_specs=out_specs_l + [ANY] * k_out, out_shape=_out_hbm(out_shape_l + list(carry.outs)),
        scratch_shapes=list(scratch) + list(carry.scratch), input_output_aliases=all_aliases,
        compiler_params=params)

    def run(*args):
        res = call(*_in_hbm(args), *carry.ins)
        core = res[:n_out]
        return (core[0] if single else list(core)), list(res[n_out:])

    return run


def _full(shape):
    nd = len(shape)
    return pl.BlockSpec(shape, lambda *_: (0,) * nd)


ANY = pl.BlockSpec(memory_space=pl.ANY)


def _mm(a, b, *, name, mode="nn", out_dtype=F32, tm, tn, tk):
    if mode == "nn":
        (m, k), (k2, n) = a.shape, b.shape
    elif mode == "nt":
        (m, k), (n, k2) = a.shape, b.shape
    else:
        (k, m), (k2, n) = a.shape, b.shape
    assert k == k2 and m % tm == 0 and n % tn == 0 and k % tk == 0, (name, a.shape, b.shape)
    nk = k // tk
    dims = {"nn": NN, "nt": NT, "tn": TN}[mode]

    def body(a_ref, b_ref, o_ref, acc):
        kk = pl.program_id(2)

        @pl.when(kk == 0)
        def _():
            acc[...] = jnp.zeros_like(acc)

        acc[...] += _bdot(a_ref[...], b_ref[...], dims)

        @pl.when(kk == nk - 1)
        def _():
            o_ref[...] = acc[...].astype(out_dtype)

    a_spec = (pl.BlockSpec((tk, tm), lambda i, j, kk: (kk, i)) if mode == "tn"
              else pl.BlockSpec((tm, tk), lambda i, j, kk: (i, kk)))
    b_spec = (pl.BlockSpec((tn, tk), lambda i, j, kk: (j, kk)) if mode == "nt"
              else pl.BlockSpec((tk, tn), lambda i, j, kk: (kk, j)))
    return _pcall(body, name=name, grid=(m // tm, n // tn, nk), in_specs=[a_spec, b_spec],
                  out_specs=pl.BlockSpec((tm, tn), lambda i, j, kk: (i, j)),
                  out_shape=jax.ShapeDtypeStruct((m, n), out_dtype),
                  scratch=[pltpu.VMEM((tm, tn), F32)])(a, b)


NT_IN = NCOL // 256


def _src_block(j):
    return j + jnp.where(j < 4, 2, jnp.where(j < 6, 3, jnp.where(j < 8, -6, jnp.where(
        j < 16, 5, jnp.where(j < 20, -7, -14)))))


def _mm_in(h, wt, tm, carry=None):
    tt = h.shape[0]

    def body(h_ref, w_ref, o_ref):
        o_ref[...] = _bdot(h_ref[...], w_ref[...], NT)

    return _pcall(body, name="mm_in", grid=(tt // tm, NT_IN),
                  in_specs=[pl.BlockSpec((tm, D), lambda i, j: (i, 0)),
                            pl.BlockSpec((256, D), lambda i, j: (_src_block(j), 0))],
                  out_specs=pl.BlockSpec((tm, 256), lambda i, j: (i, j)),
                  out_shape=jax.ShapeDtypeStruct((tt, NCOL), F32), carry=carry)(h, wt)


def _mm_dh(dp, wt, tm, carry=None):
    tt = dp.shape[0]
    per, ng = 3, NT_IN // 3

    def body(d_ref, w0, w1, w2, o_ref, acc):
        kk = pl.program_id(1)

        @pl.when(kk == 0)
        def _():
            acc[...] = jnp.zeros_like(acc)

        acc[...] += (_bdot(d_ref[:, 0:256], w0[...]) + _bdot(d_ref[:, 256:512], w1[...])
                     + _bdot(d_ref[:, 512:768], w2[...]))

        @pl.when(kk == ng - 1)
        def _():
            o_ref[...] = acc[...]

    wspecs = [pl.BlockSpec((256, D), functools.partial(lambda t, i, kk: (_src_block(per * kk + t), 0), t))
              for t in range(per)]
    return _pcall(body, name="mm_dh", grid=(tt // tm, ng),
                  in_specs=[pl.BlockSpec((tm, per * 256), lambda i, kk: (i, kk))] + wspecs,
                  out_specs=pl.BlockSpec((tm, D), lambda i, kk: (i, 0)),
                  out_shape=jax.ShapeDtypeStruct((tt, D), F32), scratch=[pltpu.VMEM((tm, D), F32)],
                  carry=carry)(dp, wt, wt, wt)


def _mm_gin(dp, h, tk):
    tt = dp.shape[0]
    nk = tt // tk

    def body(d_ref, h_ref, o_ref, acc):
        kk = pl.program_id(1)

        @pl.when(kk == 0)
        def _():
            acc[...] = jnp.zeros_like(acc)

        acc[...] += _bdot(d_ref[...], h_ref[...], TN)

        @pl.when(kk == nk - 1)
        def _():
            o_ref[...] = acc[...].astype(BF16)

    return _pcall(body, name="mm_gin", grid=(NT_IN, nk),
                  in_specs=[pl.BlockSpec((tk, 256), lambda j, kk: (kk, j)),
                            pl.BlockSpec((tk, D), lambda j, kk: (kk, 0))],
                  out_specs=pl.BlockSpec((256, D), lambda j, kk: (_src_block(j), 0)),
                  out_shape=jax.ShapeDtypeStruct((NCOL, D), BF16), scratch=[pltpu.VMEM((256, D), F32)])(dp, h)


def _tok_specs():
    assert L == TM
    return [_full((TM, D)), pl.BlockSpec((TM, D), lambda i: (jnp.maximum(i - 1, 0), 0))]


def _mod1(ctx, x, nw, ss):
    rows = L + x.shape[0]

    def body(c_ref, x_ref, nw_ref, ss_ref, h_ref):
        t = jnp.where(pl.program_id(0) == 0, c_ref[...], x_ref[...])
        r = lax.rsqrt(jnp.mean(t * t, axis=-1, keepdims=True) + EPS)
        s = ss_ref[0]
        h_ref[...] = ((t * r * nw_ref[...]) * (1.0 + s[1:2]) + s[0:1]).astype(BF16)

    return _pcall(body, name="mod1", grid=(rows // TM,),
                  in_specs=_tok_specs() + [_full((1, D)),
                                           pl.BlockSpec((1, 2, D), lambda i: (jnp.minimum(i, 1), 0, 0))],
                  out_specs=pl.BlockSpec((TM, D), lambda i: (i, 0)),
                  out_shape=jax.ShapeDtypeStruct((rows, D), BF16))(ctx, x, nw, ss)


def _norm_bwd_rows(x, dh, nw, scale):
    r = lax.rsqrt(jnp.mean(x * x, axis=-1, keepdims=True) + EPS)
    xh = x * r
    dxh = dh * ((1.0 + scale) * nw)
    dx = r * (dxh - xh * jnp.mean(dxh * xh, axis=-1, keepdims=True))
    return dx, xh


def _out_proj_mod2(mixed, w_o, x, g1, nw2, ss2):
    s_len = x.shape[0]
    tm = 512

    def body(m_ref, w_ref, x_ref, g_ref, nw_ref, ss_ref, ao_ref, x1_ref, h_ref):
        ao = _bdot(m_ref[...], w_ref[...])
        ao_ref[...] = ao
        x1 = x_ref[...] + g_ref[...] * ao
        x1_ref[...] = x1
        r = lax.rsqrt(jnp.mean(x1 * x1, axis=-1, keepdims=True) + EPS)
        s = ss_ref[0]
        h_ref[...] = ((x1 * r * nw_ref[...]) * (1.0 + s[1:2]) + s[0:1]).astype(BF16)

    row = pl.BlockSpec((tm, D), lambda i: (i, 0))
    f = jax.ShapeDtypeStruct((s_len, D), F32)
    return _pcall(body, name="out_proj_mod2", grid=(s_len // tm,),
                  in_specs=[row, _full((D, D)), row, _full((1, D)), _full((1, D)), _full((1, 2, D))],
                  out_specs=[row, row, row],
                  out_shape=[f, f, jax.ShapeDtypeStruct((s_len, D), BF16)])(mixed, w_o, x, g1, nw2, ss2)


TS = 1024


def _acc_call(body, *, name, grid, in_specs, out_specs, out_shape, acc_shapes, args, carry=None):
    return _pcall(body, name=name, grid=grid, in_specs=in_specs, out_specs=out_specs, out_shape=out_shape,
                  scratch=[pltpu.VMEM(s, F32) for s in acc_shapes], carry=carry)(*args)


def _mm_cs(a, w4, *, name):
    m, k = a.shape
    _, _, ns = w4.shape

    def body(a_ref, w_ref, o_ref):
        o_ref[...] = _bdot(a_ref[...], w_ref[0])

    return _pcall(body, name=name, grid=(m // TS, 4),
                  in_specs=[pl.BlockSpec((TS, k), lambda i, j: (i, 0)),
                            pl.BlockSpec((1, k, ns), lambda i, j: (j, 0, 0))],
                  out_specs=pl.BlockSpec((TS, ns), lambda i, j: (i, j)),
                  out_shape=jax.ShapeDtypeStruct((m, 4 * ns), F32))(a, w4)


def _mm_cs_nt(a, w4, *, name):
    m = a.shape[0]
    _, k, ns = w4.shape

    def body(a_ref, w_ref, o_ref, acc):
        j = pl.program_id(1)

        @pl.when(j == 0)
        def _():
            acc[...] = jnp.zeros_like(acc)

        acc[...] += _bdot(a_ref[...], w_ref[0], NT)

        @pl.when(j == 3)
        def _():
            o_ref[...] = acc[...]

    return _acc_call(body, name=name, grid=(m // TS, 4),
                     in_specs=[pl.BlockSpec((TS, ns), lambda i, j: (i, j)),
                               pl.BlockSpec((1, k, ns), lambda i, j: (j, 0, 0))],
                     out_specs=pl.BlockSpec((TS, k), lambda i, j: (i, 0)),
                     out_shape=jax.ShapeDtypeStruct((m, k), F32), acc_shapes=[(TS, k)], args=(a, w4))


def _mm_cs_tn(a, b, ns, *, name):
    s_len, k = a.shape
    nk = s_len // TS

    def body(a_ref, b_ref, o_ref, acc):
        t = pl.program_id(1)

        @pl.when(t == 0)
        def _():
            acc[...] = jnp.zeros_like(acc)

        acc[...] += _bdot(a_ref[...], b_ref[...], TN)

        @pl.when(t == nk - 1)
        def _():
            o_ref[0] = acc[...].astype(o_ref.dtype)

    return _acc_call(body, name=name, grid=(4, nk),
                     in_specs=[pl.BlockSpec((TS, k), lambda j, t: (t, 0)),
                               pl.BlockSpec((TS, ns), lambda j, t: (t, j))],
                     out_specs=pl.BlockSpec((1, k, ns), lambda j, t: (j, 0, 0)),
                     out_shape=jax.ShapeDtypeStruct((4, k, ns), BF16), acc_shapes=[(k, ns)], args=(a, b))


def _ffn_up(h2, g4, u4, carry=None):
    s_len = h2.shape[0]
    ns = g4.shape[1]

    def body(h_ref, g_ref, u_ref, a_ref, b_ref, z_ref):
        h = h_ref[...]
        a = _bdot(h, g_ref[0], NT)
        b = _bdot(h, u_ref[0], NT)
        a_ref[0] = a.astype(BF16)
        b_ref[0] = b.astype(BF16)
        z_ref[0] = (a * _sig(a) * b).astype(BF16)

    w = pl.BlockSpec((1, ns, D), lambda i, j: (j, 0, 0))
    o = pl.BlockSpec((1, TS, ns), lambda i, j: (j, i, 0))
    f = jax.ShapeDtypeStruct((4, s_len, ns), BF16)
    return _pcall(body, name="ffn_up", grid=(s_len // TS, 4),
                  in_specs=[pl.BlockSpec((TS, D), lambda i, j: (i, 0)), w, w], out_specs=[o, o, o],
                  out_shape=[f, f, jax.ShapeDtypeStruct((4, s_len, ns), BF16)], carry=carry)(h2, g4, u4)


def _ffn_down_loss(z4, dn4, x1, g2, tgt):
    _, s_len, ns = z4.shape

    def body(z_ref, w_ref, x1_ref, g_ref, t_ref, sq_ref, dx2_ref, dyb_ref, dg_ref, acc):
        i, j = pl.program_id(0), pl.program_id(1)

        @pl.when((i == 0) & (j == 0))
        def _():
            sq_ref[...] = jnp.zeros_like(sq_ref)
            dg_ref[...] = jnp.zeros_like(dg_ref)

        @pl.when(j == 0)
        def _():
            acc[...] = jnp.zeros_like(acc)

        acc[...] += _bdot(z_ref[0], w_ref[0])

        @pl.when(j == 3)
        def _():
            y_ = acc[...]
            g = g_ref[...]
            e = x1_ref[...] + g * y_ - t_ref[...]
            sq_ref[...] += jnp.sum(e * e, axis=0, keepdims=True)
            dx2 = e * (1.0 / D)
            dx2_ref[...] = dx2
            dyb_ref[...] = (g * dx2).astype(BF16)
            dg_ref[...] += jnp.sum(dx2 * y_, axis=0, keepdims=True)

    row = pl.BlockSpec((TS, D), lambda i, j: (i, 0))
    vec = _full((1, D))
    return _acc_call(body, name="ffn_down_loss", grid=(s_len // TS, 4),
                     in_specs=[pl.BlockSpec((1, TS, ns), lambda i, j: (j, i, 0)),
                               pl.BlockSpec((1, ns, D), lambda i, j: (j, 0, 0)), row, vec, row],
                     out_specs=[vec, row, row, vec],
                     out_shape=[jax.ShapeDtypeStruct((1, D), F32), jax.ShapeDtypeStruct((s_len, D), F32),
                                jax.ShapeDtypeStruct((s_len, D), BF16), jax.ShapeDtypeStruct((1, D), F32)],
                     acc_shapes=[(TS, D)], args=(z4, dn4, x1, g2, tgt))


def _ffn_dz(dyb, dn4, a4, b4):
    _, s_len, ns = a4.shape

    def body(dy_ref, w_ref, a_ref, b_ref, da_ref, db_ref):
        dz = _bdot(dy_ref[...], w_ref[0], NT)
        a = a_ref[0].astype(F32)
        s = _sig(a)
        da_ref[0] = (dz * b_ref[0].astype(F32) * (s * (1.0 + a * (1.0 - s)))).astype(BF16)
        db_ref[0] = (dz * (a * s)).astype(BF16)

    t = pl.BlockSpec((1, TS, ns), lambda i, j: (j, i, 0))
    o = jax.ShapeDtypeStruct((4, s_len, ns), BF16)
    return _pcall(body, name="ffn_dz", grid=(s_len // TS, 4),
                  in_specs=[pl.BlockSpec((TS, D), lambda i, j: (i, 0)),
                            pl.BlockSpec((1, ns, D), lambda i, j: (j, 0, 0)), t, t],
                  out_specs=[t, t], out_shape=[o, o])(dyb, dn4, a4, b4)


def _ffn_gdn(z4, dyb):
    _, s_len, ns = z4.shape
    nk = s_len // TS

    def body(z_ref, dy_ref, o_ref, acc):
        t = pl.program_id(1)

        @pl.when(t == 0)
        def _():
            acc[...] = jnp.zeros_like(acc)

        acc[...] += _bdot(z_ref[0], dy_ref[...], TN)

        @pl.when(t == nk - 1)
        def _():
            o_ref[0] = acc[...].astype(o_ref.dtype)

    return _acc_call(body, name="ffn_gdn", grid=(4, nk),
                     in_specs=[pl.BlockSpec((1, TS, ns), lambda j, t: (j, t, 0)),
                               pl.BlockSpec((TS, D), lambda j, t: (t, 0))],
                     out_specs=pl.BlockSpec((1, ns, D), lambda j, t: (j, 0, 0)),
                     out_shape=jax.ShapeDtypeStruct((4, ns, D), BF16), acc_shapes=[(ns, D)], args=(z4, dyb))


def _ffn_dh2(da4, db4, g4, u4, carry=None):
    _, s_len, ns = da4.shape

    def body(da_ref, db_ref, g_ref, u_ref, o_ref, acc):
        j = pl.program_id(1)

        @pl.when(j == 0)
        def _():
            acc[...] = jnp.zeros_like(acc)

        acc[...] += _bdot(da_ref[0], g_ref[0]) + _bdot(db_ref[0], u_ref[0])

        @pl.when(j == 3)
        def _():
            o_ref[...] = acc[...]

    t = pl.BlockSpec((1, TS, ns), lambda i, j: (j, i, 0))
    w = pl.BlockSpec((1, ns, D), lambda i, j: (j, 0, 0))
    return _acc_call(body, name="ffn_dh2", grid=(s_len // TS, 4), in_specs=[t, t, w, w],
                     out_specs=pl.BlockSpec((TS, D), lambda i, j: (i, 0)),
                     out_shape=jax.ShapeDtypeStruct((s_len, D), F32), acc_shapes=[(TS, D)],
                     args=(da4, db4, g4, u4), carry=carry)


def _ffn_ggu(h2, da4, db4, carry=None):
    _, s_len, ns = da4.shape
    nk = s_len // TS

    def body(h_ref, da_ref, db_ref, gg_ref, gu_ref, acc_g, acc_u):
        t = pl.program_id(1)

        @pl.when(t == 0)
        def _():
            acc_g[...] = jnp.zeros_like(acc_g)
            acc_u[...] = jnp.zeros_like(acc_u)

        h = h_ref[...]
        acc_g[...] += _bdot(da_ref[0], h, TN)
        acc_u[...] += _bdot(db_ref[0], h, TN)

        @pl.when(t == nk - 1)
        def _():
            gg_ref[0] = acc_g[...].astype(BF16)
            gu_ref[0] = acc_u[...].astype(BF16)

    d = pl.BlockSpec((1, TS, ns), lambda j, t: (j, t, 0))
    o = pl.BlockSpec((1, ns, D), lambda j, t: (j, 0, 0))
    f = jax.ShapeDtypeStruct((4, ns, D), BF16)
    return _acc_call(body, name="ffn_ggu", grid=(4, nk),
                     in_specs=[pl.BlockSpec((TS, D), lambda j, t: (t, 0)), d, d], out_specs=[o, o],
                     out_shape=[f, f], acc_shapes=[(ns, D), (ns, D)], args=(h2, da4, db4), carry=carry)


def _mod2_bwd(x1, dh2, dx2, ao, nw2, ss2, g1):
    s_len = x1.shape[0]

    def body(x1_ref, dh_ref, dx2_ref, ao_ref, nw_ref, ss_ref, g_ref,
             dx1_ref, da_ref, dss_ref, dnw_ref, dg_ref):
        i = pl.program_id(0)

        @pl.when(i == 0)
        def _():
            dss_ref[...] = jnp.zeros_like(dss_ref)
            dnw_ref[...] = jnp.zeros_like(dnw_ref)
            dg_ref[...] = jnp.zeros_like(dg_ref)

        dh = dh_ref[...]
        nw = nw_ref[...]
        scale = ss_ref[0][1:2]
        dxn, xh = _norm_bwd_rows(x1_ref[...], dh, nw, scale)
        dx1 = dx2_ref[...] + dxn
        dx1_ref[...] = dx1
        da_ref[...] = (g_ref[...] * dx1).astype(BF16)
        dg_ref[...] += jnp.sum(dx1 * ao_ref[...], axis=0, keepdims=True)
        dsh = jnp.sum(dh, axis=0, keepdims=True)
        dsc = jnp.sum(dh * xh * nw, axis=0, keepdims=True)
        dss_ref[...] += jnp.concatenate([dsh, dsc], axis=0)
        dnw_ref[...] += jnp.sum(dh * xh * (1.0 + scale), axis=0, keepdims=True)

    row = pl.BlockSpec((TM, D), lambda i: (i, 0))
    vec = _full((1, D))
    return _pcall(body, name="mod2_bwd", grid=(s_len // TM,),
                  in_specs=[row, row, row, row, vec, _full((1, 2, D)), vec],
                  out_specs=[row, row, _full((2, D)), vec, vec],
                  out_shape=[jax.ShapeDtypeStruct((s_len, D), F32), jax.ShapeDtypeStruct((s_len, D), BF16),
                             jax.ShapeDtypeStruct((2, D), F32), jax.ShapeDtypeStruct((1, D), F32),
                             jax.ShapeDtypeStruct((1, D), F32)])(x1, dh2, dx2, ao, nw2, ss2, g1)


def _mod1_bwd(ctx, x, dh, dx1, nw1, ss1, carry=None):
    s_len = dx1.shape[0]
    tt = L + s_len

    def body(c_ref, x_ref, dh_ref, dx1_ref, nw_ref, ss_ref, dx_ref, dss_ref, dnw_ref):
        i = pl.program_id(0)
        tok = jnp.where(i == 0, c_ref[...], x_ref[...])

        @pl.when(i == 0)
        def _():
            dnw_ref[...] = jnp.zeros_like(dnw_ref)

        @pl.when(i <= 1)
        def _():
            dss_ref[...] = jnp.zeros_like(dss_ref)

        dh_ = dh_ref[...]
        nw = nw_ref[...]
        scale = ss_ref[0][1:2]
        dxn, xh = _norm_bwd_rows(tok, dh_, nw, scale)

        @pl.when(i >= 1)
        def _():
            dx_ref[...] = dx1_ref[...] + dxn

        dsh = jnp.sum(dh_, axis=0, keepdims=True)
        dsc = jnp.sum(dh_ * xh * nw, axis=0, keepdims=True)
        dss_ref[...] += jnp.concatenate([dsh, dsc], axis=0)[None]
        dnw_ref[...] += jnp.sum(dh_ * xh * (1.0 + scale), axis=0, keepdims=True)

    row = pl.BlockSpec((TM, D), lambda i: (i, 0))
    lat = pl.BlockSpec((TM, D), lambda i: (jnp.maximum(i - 1, 0), 0))
    sel = pl.BlockSpec((1, 2, D), lambda i: (jnp.minimum(i, 1), 0, 0))
    return _pcall(body, name="mod1_bwd", grid=(tt // TM,),
                  in_specs=_tok_specs() + [row, lat, _full((1, D)), sel],
                  out_specs=[lat, sel, _full((1, D))],
                  out_shape=[jax.ShapeDtypeStruct((s_len, D), F32), jax.ShapeDtypeStruct((2, 2, D), F32),
                             jax.ShapeDtypeStruct((1, D), F32)], carry=carry)(ctx, x, dh, dx1, nw1, ss1)


def _rows(c):
    return slice(c * CH, (c + 1) * CH)


def _chunk_masks(rev, transpose=False):
    r = lax.broadcasted_iota(jnp.int32, (TM, TM), 0)
    c = lax.broadcasted_iota(jnp.int32, (TM, TM), 1)
    same = (r // CH) == (c // CH)
    before = (c >= r) if (rev != transpose) else (c <= r)
    return same & before, same


def _chunk_scan(x, rev, transpose=False):
    r = lax.broadcasted_iota(jnp.int32, (CH, CH), 0)
    c = lax.broadcasted_iota(jnp.int32, (CH, CH), 1)
    tri = ((c >= r) if (rev != transpose) else (c <= r)).astype(F32)
    return jnp.concatenate([_dot(tri, x[_rows(ch)], prec=HI) for ch in range(x.shape[0] // CH)], axis=0)


def _chunk_total(x):
    return jnp.concatenate([jnp.broadcast_to(jnp.sum(x[_rows(ch)], axis=0, keepdims=True), (CH, x.shape[1]))
                            for ch in range(x.shape[0] // CH)], axis=0)


def _hgrn_gate(fl, qraw, lg):
    lb = 1.0 / (1.0 + jnp.exp(lg[1:2] - lg[0:1]))
    sg = _sig(fl)
    f = lb + (1.0 - lb) * sg
    q = qraw * _sig(qraw) * (HGD ** -0.5)
    return lb, sg, f, q


def _hgrn_fwd(p, lg, *, rev, carry=None):
    tt = p.shape[0]
    nt = tt // TM
    ncht = TM // CH
    d = 1 if rev else 0

    def tile_of(s):
        return jnp.where(s == 0, 0, nt - s) if rev else s

    def body(f_ref, inp_ref, q_ref, lg_ref, o_ref, st_ref, state):
        s = pl.program_id(0)

        @pl.when(s == 0)
        def _():
            state[...] = jnp.zeros_like(state)

        _, _, f, q = _hgrn_gate(f_ref[...], q_ref[...], lg_ref[0])
        lf = jnp.log(f)
        causal, _ = _chunk_masks(rev)
        cum = _chunk_scan(lf, rev)
        tot = _chunk_total(lf)
        qd = (q * jnp.exp(cum)).astype(BF16)
        kd = ((1.0 - f) * jnp.exp(-cum)).astype(BF16)
        ke = ((1.0 - f) * jnp.exp(tot - cum)).astype(BF16)
        et = jnp.exp(tot)
        v = inp_ref[...].astype(BF16)
        order = range(ncht - 1, -1, -1) if rev else range(ncht)
        outs = []
        for h in range(4):
            sl = slice(h * HGD, (h + 1) * HGD)
            qd_, kd_, ke_, v_ = qd[:, sl], kd[:, sl], ke[:, sl], v[:, sl]
            pm = jnp.where(causal, _dot(qd_, kd_, NT), 0.0).astype(BF16)
            o_h = _dot(pm, v_)
            upd = [_dot(v_[_rows(c)], ke_[_rows(c)], TN) for c in range(ncht)]
            st = state[h]
            for c in order:
                st_ref[c, h] = st
                st = st * et[c * CH:c * CH + 1, sl] + upd[c]
            state[h] = st
            inter = [_dot(qd_[_rows(c)], st_ref[c, h].astype(BF16), NT) for c in range(ncht)]
            outs.append(o_h + jnp.concatenate(inter, axis=0))
        o_ref[...] = jnp.concatenate(outs, axis=1)

    def col(cb):
        return pl.BlockSpec((TM, HGW), lambda s: (tile_of(s), cb))

    return _pcall(
        body, name="hgrn_fwd_rev" if rev else "hgrn_fwd", grid=(nt,),
        in_specs=[col(C_FB if rev else C_FF), col(C_INP), col(C_QHG),
                  pl.BlockSpec((1, 2, HGW), lambda s: (d, 0, 0))],
        out_specs=[pl.BlockSpec((TM, HGW), lambda s: (tile_of(s), 0)),
                   pl.BlockSpec((ncht, 4, HGD, HGD), lambda s: (tile_of(s), 0, 0, 0))],
        out_shape=[jax.ShapeDtypeStruct((tt, HGW), F32),
                   jax.ShapeDtypeStruct((nt * ncht, 4, HGD, HGD), F32)],
        scratch=[pltpu.VMEM((4, HGD, HGD), F32)], carry=carry)(p, p, p, lg)


def _hgrn_bwd(p, lg, do, st, dp, prev, *, rev, carry=None):
    tt = p.shape[0]
    nt = tt // TM
    ncht = TM // CH
    d = 1 if rev else 0
    second = prev is not None

    def tile_of(s):
        return jnp.where(s == nt - 1, 0, s + 1) if rev else nt - 1 - s

    def body(*refs):
        if second:
            (f_ref, inp_ref, q_ref, lg_ref, do_ref, st_ref, dvp_ref, dqp_ref, _dp_in,
             dp_ref, dlg_ref, dstate) = refs
        else:
            (f_ref, inp_ref, q_ref, lg_ref, do_ref, st_ref, _dp_in,
             dp_ref, dv_ref, dq_ref, dlg_ref, dstate) = refs
        s = pl.program_id(0)
        tile = tile_of(s)

        @pl.when(s == 0)
        def _():
            dstate[...] = jnp.zeros_like(dstate)
            dlg_ref[...] = jnp.zeros_like(dlg_ref)

        qraw = q_ref[...]
        lb, sg, f, q = _hgrn_gate(f_ref[...], qraw, lg_ref[0])
        lf = jnp.log(f)
        causal, _ = _chunk_masks(rev)
        causal_t, _ = _chunk_masks(rev, transpose=True)
        cum = _chunk_scan(lf, rev)
        tot = _chunk_total(lf)
        ea, eb, ee, et = jnp.exp(cum), jnp.exp(-cum), jnp.exp(tot - cum), jnp.exp(tot)
        qdf, kdf, kef = q * ea, (1.0 - f) * eb, (1.0 - f) * ee
        qd, kd, ke = qdf.astype(BF16), kdf.astype(BF16), kef.astype(BF16)
        v = inp_ref[...].astype(BF16)
        dob = jnp.where(tile == 0, 0.0, do_ref[...]).astype(BF16)
        order = range(ncht) if rev else range(ncht - 1, -1, -1)
        dq_l, dk_l, dv_l, dcum_l, dtot_l = [], [], [], [], []
        for h in range(4):
            sl = slice(h * HGD, (h + 1) * HGD)
            qd_, kd_, ke_, v_, do_ = qd[:, sl], kd[:, sl], ke[:, sl], v[:, sl], dob[:, sl]
            pmt = jnp.where(causal_t, _dot(kd_, qd_, NT), 0.0).astype(BF16)
            dpm = jnp.where(causal, _dot(do_, v_, NT), 0.0).astype(BF16)
            dpmt = jnp.where(causal_t, _dot(v_, do_, NT), 0.0).astype(BF16)
            dv = _dot(pmt, do_)
            dqd = _dot(dpm, kd_)
            dkd = _dot(dpmt, qd_)
            upd = [_dot(do_[_rows(c)], qd_[_rows(c)], TN) for c in range(ncht)]
            ds = dstate[h]
            ds1 = [None] * ncht
            for c in order:
                ds1[c] = ds
                ds = ds * et[c * CH:c * CH + 1, sl] + upd[c]
            dstate[h] = ds
            dke_c, dv_c, dqd_c, dtot_c = [], [], [], []
            for c in range(ncht):
                st0 = st_ref[c, h]
                dsb = ds1[c].astype(BF16)
                dke_ = _dot(v_[_rows(c)], dsb)
                dke_c.append(dke_)
                dv_c.append(_dot(ke_[_rows(c)], dsb, NT))
                dqd_c.append(_dot(do_[_rows(c)], st0.astype(BF16)))
                dt = (jnp.sum(ds1[c] * st0, axis=0, keepdims=True) * et[c * CH:c * CH + 1, sl]
                      + jnp.sum(dke_ * kef[_rows(c), sl], axis=0, keepdims=True))
                dtot_c.append(jnp.broadcast_to(dt, (CH, HGD)))
            dke = jnp.concatenate(dke_c, axis=0)
            dqd = dqd + jnp.concatenate(dqd_c, axis=0)
            dv_l.append(dv + jnp.concatenate(dv_c, axis=0))
            dtot_l.append(jnp.concatenate(dtot_c, axis=0))
            dq_l.append(dqd * ea[:, sl])
            dk_l.append(dkd * eb[:, sl] + dke * ee[:, sl])
            dcum_l.append(dqd * qdf[:, sl] - dkd * kdf[:, sl] - dke * kef[:, sl])
        dcum = jnp.concatenate(dcum_l, axis=1)
        dlf = _chunk_scan(dcum, rev, transpose=True) + jnp.concatenate(dtot_l, axis=1)
        dq_t = jnp.concatenate(dq_l, axis=1)
        dv_t = jnp.concatenate(dv_l, axis=1)

        df = dlf / f - jnp.concatenate(dk_l, axis=1)
        dfl = df * (1.0 - lb) * sg * (1.0 - sg)
        dlb = jnp.sum(df * (1.0 - sg), axis=0, keepdims=True)
        dl0 = dlb * lb * (1.0 - lb)
        dlg_ref[...] += jnp.concatenate([dl0, -dl0], axis=0)[None]
        if second:
            sq = _sig(qraw)
            dqr = (dqp_ref[...] + dq_t) * (HGD ** -0.5) * (sq * (1.0 + qraw * (1.0 - sq)))
            dp_ref[...] = jnp.concatenate([dfl, dvp_ref[...] + dv_t, dqr], axis=1).astype(BF16)
        else:
            dp_ref[...] = dfl.astype(BF16)
            dv_ref[...] = dv_t
            dq_ref[...] = dq_t

    def col(cb):
        return pl.BlockSpec((TM, HGW), lambda s: (tile_of(s), cb))

    tok = pl.BlockSpec((TM, HGW), lambda s: (tile_of(s), 0))
    in_specs = [col(C_FB if rev else C_FF), col(C_INP), col(C_QHG),
                pl.BlockSpec((1, 2, HGW), lambda s: (d, 0, 0)),
                pl.BlockSpec((TM, HGW), lambda s: (jnp.maximum(tile_of(s) - 1, 0), 0)),
                pl.BlockSpec((ncht, 4, HGD, HGD), lambda s: (tile_of(s), 0, 0, 0))]
    args = [p, p, p, lg, do, st]
    dlg_spec = _full((1, 2, HGW))
    dlg_shape = jax.ShapeDtypeStruct((1, 2, HGW), F32)
    if second:
        in_specs += [tok, tok]
        args += [prev[0], prev[1]]
        out_specs = [pl.BlockSpec((TM, 3 * HGW), lambda s: (tile_of(s), 0)), dlg_spec]
        out_shape = [jax.ShapeDtypeStruct(dp.shape, BF16), dlg_shape]
    else:
        out_specs = [pl.BlockSpec((TM, HGW), lambda s: (tile_of(s), C_FB if rev else C_FF)), tok, tok, dlg_spec]
        out_shape = [jax.ShapeDtypeStruct(dp.shape, BF16), jax.ShapeDtypeStruct((tt, HGW), F32),
                     jax.ShapeDtypeStruct((tt, HGW), F32), dlg_shape]
    in_specs.append(ANY)
    args.append(dp)
    return _pcall(body, name="hgrn_bwd_rev" if rev else "hgrn_bwd", grid=(nt,),
                  in_specs=in_specs, out_specs=out_specs, out_shape=out_shape,
                  scratch=[pltpu.VMEM((4, HGD, HGD), F32)],
                  aliases={len(args) - 1: 0}, carry=carry)(*args)


def _head_rms(o, w, nheads):
    outs = []
    for h in range(nheads):
        oh = o[:, h * HGD:(h + 1) * HGD]
        outs.append(oh * lax.rsqrt(jnp.mean(oh * oh, axis=-1, keepdims=True) + EPS))
    return jnp.concatenate(outs, axis=1)


def _readout(o0, o1, p, hw4):
    s_len = o0.shape[0] - L

    def body(o0_ref, o1_ref, g_ref, w_ref, y_ref):
        xh = _head_rms(o0_ref[...] + o1_ref[...], None, 4)
        g = g_ref[...]
        y_ref[...] = (xh * w_ref[...] * (g * _sig(g))).astype(BF16)

    lat = pl.BlockSpec((TM, HGW), lambda i: (i + 1, 0))
    return _pcall(body, name="readout", grid=(s_len // TM,),
                  in_specs=[lat, lat, pl.BlockSpec((TM, HGW), lambda i: (i + 1, C_GHG)), _full((1, HGW))],
                  out_specs=pl.BlockSpec((TM, HGW), lambda i: (i, 0)),
                  out_shape=jax.ShapeDtypeStruct((s_len, HGW), BF16))(o0, o1, p, hw4)


def _readout_bwd(o0, o1, p, hw4, dy, dp, carry=None):
    tt = o0.shape[0]
    s_len = tt - L

    def body(o0_ref, o1_ref, g_ref, w_ref, dy_ref, _dp_in, dp_ref, do_ref, dw_ref):
        i = pl.program_id(0)

        @pl.when(i == 0)
        def _():
            dw_ref[...] = jnp.zeros_like(dw_ref)
            dp_ref[...] = jnp.zeros_like(dp_ref)

        @pl.when(i >= 1)
        def _():
            o = o0_ref[...] + o1_ref[...]
            g = g_ref[...]
            w = w_ref[...]
            sg = _sig(g)
            dy_ = dy_ref[...]
            dsw = dy_ * (g * sg)
            outs, xhs = [], []
            for h in range(4):
                sl = slice(h * HGD, (h + 1) * HGD)
                oh = o[:, sl]
                r = lax.rsqrt(jnp.mean(oh * oh, axis=-1, keepdims=True) + EPS)
                xh = oh * r
                dxh = dsw[:, sl] * w[:, sl]
                outs.append(r * (dxh - xh * jnp.mean(dxh * xh, axis=-1, keepdims=True)))
                xhs.append(xh)
            xh = jnp.concatenate(xhs, axis=1)
            do_ref[...] = jnp.concatenate(outs, axis=1)
            dp_ref[...] = (dy_ * xh * w * (sg * (1.0 + g * (1.0 - sg)))).astype(BF16)
            dw_ref[...] += jnp.sum(dsw * xh, axis=0, keepdims=True)

    tok = pl.BlockSpec((TM, HGW), lambda i: (i, 0))
    lat = pl.BlockSpec((TM, HGW), lambda i: (jnp.maximum(i - 1, 0), 0))
    return _pcall(body, name="readout_bwd", grid=(tt // TM,),
                  in_specs=[tok, tok, pl.BlockSpec((TM, HGW), lambda i: (i, C_GHG)), _full((1, HGW)), lat, ANY],
                  out_specs=[pl.BlockSpec((TM, HGW), lambda i: (i, C_GHG)), lat, _full((1, HGW))],
                  out_shape=[jax.ShapeDtypeStruct(dp.shape, BF16), jax.ShapeDtypeStruct((s_len, HGW), F32),
                             jax.ShapeDtypeStruct((1, HGW), F32)],
                  aliases={5: 0}, carry=carry)(o0, o1, p, hw4, dy, dp)


def _rope_tables(s_len):
    t = np.arange(s_len)
    inv = ROPE_THETA ** (-np.arange(0, 32, 2, dtype=np.float64) / 32)
    def half(pos):
        ang = pos[:, None].astype(np.float64) * inv[None, :]
        return (np.concatenate([np.cos(ang), np.cos(ang)], 1), np.concatenate([-np.sin(ang), np.sin(ang)], 1))
    cr, sr = half(t // GRID_W)
    cc, sc = half(t % GRID_W)
    cos = np.concatenate([cr, cc, cr, cc], 1)
    sin = np.concatenate([sr, sc, sr, sc], 1)
    cos = np.concatenate([np.ones((L, 128)), cos], 0)
    sin = np.concatenate([np.zeros((L, 128)), sin], 0)
    return jnp.asarray(cos, F32), jnp.asarray(sin, F32)


def _blockdiag(n, w):
    i = np.arange(n)
    return jnp.asarray((i[:, None] // w == i[None, :] // w) / float(w), F32)


def _dup_matrix():
    m = np.zeros((128, 512), np.float32)
    for g in range(2):
        for j in range(4):
            for dd in range(HDIM):
                m[64 * g + dd, 256 * g + 64 * j + dd] = 1.0
    return m


def _head_mean(x, blockdiag):
    return _dot(x, blockdiag, prec=lax.Precision.HIGH)


def _rot(x):
    n = x.shape[1]
    lane = lax.broadcasted_iota(jnp.int32, x.shape, 1)
    return jnp.where((lane % 32) < 16, pltpu.roll(x, n - 16, 1), pltpu.roll(x, 16, 1))


def _qk_prep(p, cos, sin, qnw8, knw2, bd512, bd128, dup):
    tt = p.shape[0]

    def body(q_ref, kv_ref, cos_ref, sin_ref, qw_ref, kw_ref, b5_ref, b1_ref, dup_ref,
             qr_ref, k4_ref, v4_ref):
        cos_, sin_ = cos_ref[...], sin_ref[...]
        q = q_ref[...]
        qn = q * lax.rsqrt(_head_mean(q * q, b5_ref[...]) + EPS) * qw_ref[...]
        cos4 = jnp.concatenate([cos_] * 4, axis=1)
        sin4 = jnp.concatenate([sin_] * 4, axis=1)
        qr_ref[...] = ((qn * cos4 + _rot(qn) * sin4) * (HDIM ** -0.5)).astype(BF16)
        kv = kv_ref[...]
        k, v = kv[:, :128], kv[:, 128:]
        kn = k * lax.rsqrt(_head_mean(k * k, b1_ref[...]) + EPS) * kw_ref[...]
        kr = kn * cos_ + _rot(kn) * sin_
        k4_ref[...] = _bdot(kr, dup_ref[...]).astype(BF16)
        v4_ref[...] = _bdot(v, dup_ref[...]).astype(BF16)

    row = lambda w, cb: pl.BlockSpec((TM, w), lambda i: (i, cb))
    out = jax.ShapeDtypeStruct((tt, ATW), BF16)
    return _pcall(body, name="qk_prep", grid=(tt // TM,),
                  in_specs=[row(ATW, C_QRAW), row(256, C_KV), row(128, 0), row(128, 0),
                            _full((1, ATW)), _full((1, 128)), _full((ATW, ATW)), _full((128, 128)),
                            _full((128, ATW))],
                  out_specs=[row(ATW, 0)] * 3, out_shape=[out] * 3)(
                      p, p, cos, sin, qnw8, knw2, bd512, bd128, dup)


def _attn_masks(i, nb):
    r = lax.broadcasted_iota(jnp.int32, (4 * BLK, 3 * BLK + L), 0) % BLK
    c = lax.broadcasted_iota(jnp.int32, (4 * BLK, 3 * BLK + L), 1)
    kpos = (i - 1) * BLK + c
    loc = (jnp.abs(c - BLK - r) <= BLK) & (kpos >= 0) & (kpos < nb * BLK)
    return loc | (c >= 3 * BLK)


def _stack_mask():
    r = lax.broadcasted_iota(jnp.int32, (4 * BLK, 256), 0)
    lane = lax.broadcasted_iota(jnp.int32, (4 * BLK, 256), 1)
    return (r // BLK) == (lane // HDIM)


def _stack_heads(xg, fill=0.0):
    x4 = jnp.concatenate([xg] * 4, axis=0)
    return jnp.where(_stack_mask(), x4, jnp.full_like(x4, fill))


def _unstack_heads(x4):
    out = jnp.where(_lane_mask(0), x4[0:BLK], 0.0)
    for j in range(1, 4):
        out = out + jnp.where(_lane_mask(j), x4[j * BLK:(j + 1) * BLK], 0.0)
    return out


def _per_head_rows(vals):
    return jnp.concatenate([jnp.broadcast_to(v, (BLK, 1)) for v in vals], axis=0)


def _lane_mask(j):
    lane = lax.broadcasted_iota(jnp.int32, (1, 256), 1)
    return (lane // HDIM) == j


def _attn_specs(nb):
    blk = lambda off: pl.BlockSpec((BLK, ATW), lambda i: (jnp.clip(i + off, 0, nb - 1) + 2, 0))
    ctx = pl.BlockSpec((L, ATW), lambda i: (0, 0))
    return blk, ctx


def _attn_fwd(qr, k4, v4, sinks, carry=None):
    tt = qr.shape[0]
    s_len = tt - L
    nb = s_len // BLK

    def body(sk_ref, q_ref, kp, ko, kn, kc, vp, vo, vn, vc, y_ref, lse_ref):
        i = pl.program_id(0)
        valid = _attn_masks(i, nb)
        q = q_ref[...]
        ys, lses = [], []
        for g in range(2):
            gs = slice(256 * g, 256 * g + 256)
            kcat = jnp.concatenate([kp[:, gs], ko[:, gs], kn[:, gs], kc[:, gs]], axis=0)
            vcat = jnp.concatenate([vp[:, gs], vo[:, gs], vn[:, gs], vc[:, gs]], axis=0)
            sink = _per_head_rows([sk_ref[4 * g + j] for j in range(4)])
            s = jnp.where(valid, _dot(_stack_heads(q[:, gs]), kcat, NT), -1e30)
            m = jnp.maximum(jnp.max(s, axis=-1, keepdims=True), sink)
            e = jnp.exp(s - m)
            den = jnp.sum(e, axis=-1, keepdims=True) + jnp.exp(sink - m)
            ys.append(_unstack_heads(_bdot(e * (1.0 / den), vcat)))
            lses.append(_unstack_heads(jnp.broadcast_to(m + jnp.log(den), (4 * BLK, 256))))
        y_ref[...] = jnp.concatenate(ys, axis=1).astype(BF16)
        lse_ref[...] = jnp.concatenate(lses, axis=1)

    blk, ctx = _attn_specs(nb)
    out = pl.BlockSpec((BLK, ATW), lambda i: (i, 0))
    return _pcall(body, name="attn_fwd", grid=(nb,),
                  in_specs=[pl.BlockSpec(memory_space=pltpu.SMEM), blk(0),
                            blk(-1), blk(0), blk(1), ctx, blk(-1), blk(0), blk(1), ctx],
                  out_specs=[out, out],
                  out_shape=[jax.ShapeDtypeStruct((s_len, ATW), BF16),
                             jax.ShapeDtypeStruct((s_len, ATW), F32)], carry=carry)(
                      sinks, qr, k4, k4, k4, k4, v4, v4, v4, v4)


def _attn_bwd(qr, k4, v4, sinks, y, lse, dy, carry=None):
    tt = qr.shape[0]
    s_len = tt - L
    nb = s_len // BLK

    def body(sk_ref, q_ref, kp, ko, kn, kc, vp, vo, vn, vc, y_ref, lse_ref, dy_ref,
             dq_ref, dkw_ref, dvw_ref, dkc_ref, dvc_ref, dsk_ref):
        i = pl.program_id(0)

        @pl.when(i == 0)
        def _():
            dkc_ref[...] = jnp.zeros_like(dkc_ref)
            dvc_ref[...] = jnp.zeros_like(dvc_ref)
            dsk_ref[...] = jnp.zeros_like(dsk_ref)

        valid = _attn_masks(i, nb)
        q = q_ref[...]
        dy_ = dy_ref[...]
        dly = dy_ * y_ref[...].astype(F32)
        lse_ = lse_ref[...]
        dqs = []
        for g in range(2):
            gs = slice(256 * g, 256 * g + 256)
            kcat = jnp.concatenate([kp[:, gs], ko[:, gs], kn[:, gs], kc[:, gs]], axis=0)
            vcat = jnp.concatenate([vp[:, gs], vo[:, gs], vn[:, gs], vc[:, gs]], axis=0)
            q4 = _stack_heads(q[:, gs])
            dy4 = _stack_heads(dy_[:, gs]).astype(BF16)
            lse4 = jnp.max(_stack_heads(lse_[:, gs], fill=-1e30), axis=-1, keepdims=True)
            delta = jnp.sum(_stack_heads(dly[:, gs]), axis=-1, keepdims=True)
            sink = _per_head_rows([sk_ref[4 * g + j] for j in range(4)])
            pr = jnp.where(valid, jnp.exp(_dot(q4, kcat, NT) - lse4), 0.0)
            dsb = (pr * (_dot(dy4, vcat, NT) - delta)).astype(BF16)
            dsink = jnp.exp(sink - lse4) * delta
            for j in range(4):
                dsk_ref[4 * g + j:4 * g + j + 1, :] += jnp.broadcast_to(
                    -jnp.sum(dsink[j * BLK:(j + 1) * BLK], axis=0, keepdims=True), (1, 128))
            dqs.append(_unstack_heads(_dot(dsb, kcat)))
            dkg = _dot(dsb, q4, TN)
            dvg = _dot(pr.astype(BF16), dy4, TN)
            dkw_ref[0, :, gs] = dkg[:3 * BLK]
            dvw_ref[0, :, gs] = dvg[:3 * BLK]
            dkc_ref[:, gs] += dkg[3 * BLK:]
            dvc_ref[:, gs] += dvg[3 * BLK:]
        dq_ref[...] = jnp.concatenate(dqs, axis=1)

    blk, ctx = _attn_specs(nb)
    out = pl.BlockSpec((BLK, ATW), lambda i: (i, 0))
    win = pl.BlockSpec((1, 3 * BLK, ATW), lambda i: (i, 0, 0))
    acc = _full((L, ATW))
    return _pcall(body, name="attn_bwd", grid=(nb,),
                  in_specs=[pl.BlockSpec(memory_space=pltpu.SMEM), blk(0),
                            blk(-1), blk(0), blk(1), ctx, blk(-1), blk(0), blk(1), ctx, out, out, out],
                  out_specs=[out, win, win, acc, acc, _full((8, 128))],
                  out_shape=[jax.ShapeDtypeStruct((s_len, ATW), F32),
                             jax.ShapeDtypeStruct((nb, 3 * BLK, ATW), F32),
                             jax.ShapeDtypeStruct((nb, 3 * BLK, ATW), F32),
                             jax.ShapeDtypeStruct((L, ATW), F32), jax.ShapeDtypeStruct((L, ATW), F32),
                             jax.ShapeDtypeStruct((8, 128), F32)], carry=carry)(
                      sinks, qr, k4, k4, k4, k4, v4, v4, v4, v4, y, lse, dy)


def _attn_post(p, cos, sin, qnw8, knw2, bd512, bd128, dupt, dq, dkw, dvw, dkc, dvc, dp, carry=None):
    tt = p.shape[0]
    s_len = tt - L
    nb = s_len // BLK
    nctx = L // BLK

    def body(q_ref, kv_ref, cos_ref, sin_ref, qw_ref, kw_ref, b5_ref, b1_ref, dupt_ref,
             dq_ref, kwp, kwo, kwn, vwp, vwo, vwn, dkc_ref, dvc_ref, _dp_in,
             dp_ref, dqw_ref, dkw_ref):
        t = pl.program_id(0)
        j = t - nctx

        @pl.when(t == 0)
        def _():
            dqw_ref[...] = jnp.zeros_like(dqw_ref)
            dkw_ref[...] = jnp.zeros_like(dkw_ref)

        is_lat = t >= nctx
        cos_, sin_ = cos_ref[...], sin_ref[...]
        has_p = is_lat & (j >= 1)
        has_n = is_lat & (j <= nb - 2)
        dk4 = (jnp.where(is_lat, kwo[0], dkc_ref[...]) + jnp.where(has_p, kwp[0], 0.0)
               + jnp.where(has_n, kwn[0], 0.0))
        dv4 = (jnp.where(is_lat, vwo[0], dvc_ref[...]) + jnp.where(has_p, vwp[0], 0.0)
               + jnp.where(has_n, vwn[0], 0.0))
        dkr = _dot(dk4, dupt_ref[...], prec=HI)
        dv = _dot(dv4, dupt_ref[...], prec=HI)
        kv = kv_ref[...]
        k = kv[:, :128]
        kw = kw_ref[...]
        rk = lax.rsqrt(_head_mean(k * k, b1_ref[...]) + EPS)
        xk = k * rk
        dkn = dkr * cos_ + _rot(dkr * sin_)
        dxk = dkn * kw
        dk = rk * (dxk - xk * _head_mean(dxk * xk, b1_ref[...]))
        dkw_ref[...] += jnp.sum(dkn * xk, axis=0, keepdims=True)
        q = q_ref[...]
        qw = qw_ref[...]
        rq = lax.rsqrt(_head_mean(q * q, b5_ref[...]) + EPS)
        xq = q * rq
        cos4 = jnp.concatenate([cos_] * 4, axis=1)
        sin4 = jnp.concatenate([sin_] * 4, axis=1)
        dqr = jnp.where(is_lat, dq_ref[...], 0.0) * (HDIM ** -0.5)
        dqn = dqr * cos4 + _rot(dqr * sin4)
        dxq = dqn * qw
        dqraw = rq * (dxq - xq * _head_mean(dxq * xq, b5_ref[...]))
        dqw_ref[...] += jnp.sum(dqn * xq, axis=0, keepdims=True)
        dp_ref[...] = jnp.concatenate([dqraw, dk, dv], axis=1).astype(BF16)

    row = lambda w, cb: pl.BlockSpec((BLK, w), lambda t: (t, cb))
    lat = pl.BlockSpec((BLK, ATW), lambda t: (jnp.maximum(t - nctx, 0), 0))

    def part(off):
        return pl.BlockSpec((1, BLK, ATW), lambda t: (jnp.clip(t - nctx + off, 0, nb - 1), 1 - off, 0))

    cacc = pl.BlockSpec((BLK, ATW), lambda t: (jnp.minimum(t, nctx - 1), 0))
    return _pcall(body, name="attn_post", grid=(tt // BLK,),
                  in_specs=[row(ATW, C_QRAW), row(256, C_KV), row(128, 0), row(128, 0),
                            _full((1, ATW)), _full((1, 128)), _full((ATW, ATW)), _full((128, 128)),
                            _full((ATW, 128)), lat, part(-1), part(0), part(1), part(-1), part(0), part(1),
                            cacc, cacc, ANY],
                  out_specs=[pl.BlockSpec((BLK, 768), lambda t: (t, C_QKV)), _full((1, ATW)), _full((1, 128))],
                  out_shape=[jax.ShapeDtypeStruct(dp.shape, BF16), jax.ShapeDtypeStruct((1, ATW), F32),
                             jax.ShapeDtypeStruct((1, 128), F32)],
                  aliases={18: 0}, carry=carry)(p, p, cos, sin, qnw8, knw2, bd512, bd128, dupt,
                                   dq, dkw, dkw, dkw, dvw, dvw, dvw, dkc, dvc, dp)


def _merge(ah, aa, p):
    s_len = ah.shape[0]

    def body(ah_ref, aa_ref, gh_ref, ga_ref, m_ref):
        m_ref[...] = (_sig(gh_ref[...]) * ah_ref[...] + _sig(ga_ref[...]) * aa_ref[...]).astype(BF16)

    row = pl.BlockSpec((TM, D), lambda i: (i, 0))
    return _pcall(body, name="merge", grid=(s_len // TM,),
                  in_specs=[row, row, pl.BlockSpec((TM, D), lambda i: (i + 1, 2)),
                            pl.BlockSpec((TM, D), lambda i: (i + 1, 3))],
                  out_specs=row, out_shape=jax.ShapeDtypeStruct((s_len, D), BF16))(ah, aa, p, p)


def _merge_bwd(dm, ah, aa, p, carry=None):
    tt = p.shape[0]
    s_len = tt - L

    def body(dm_ref, ah_ref, aa_ref, gh_ref, ga_ref, dp_ref, dmh_ref, dma_ref):
        i = pl.program_id(0)

        @pl.when(i == 0)
        def _():
            dp_ref[...] = jnp.zeros_like(dp_ref)

        @pl.when(i >= 1)
        def _():
            dm_ = dm_ref[...]
            sh, sa = _sig(gh_ref[...]), _sig(ga_ref[...])
            dp_ref[...] = jnp.concatenate([dm_ * ah_ref[...] * sh * (1.0 - sh),
                                           dm_ * aa_ref[...] * sa * (1.0 - sa)], axis=1).astype(BF16)
            dmh_ref[...] = (dm_ * sh).astype(BF16)
            dma_ref[...] = (dm_ * sa).astype(BF16)

    lat = pl.BlockSpec((TM, D), lambda i: (jnp.maximum(i - 1, 0), 0))
    return _pcall(body, name="merge_bwd", grid=(tt // TM,),
                  in_specs=[lat, lat, lat, pl.BlockSpec((TM, D), lambda i: (i, 2)),
                            pl.BlockSpec((TM, D), lambda i: (i, 3))],
                  out_specs=[pl.BlockSpec((TM, 2 * D), lambda i: (i, C_GATES)), lat, lat],
                  out_shape=[jax.ShapeDtypeStruct((tt, NCOL), BF16), jax.ShapeDtypeStruct((s_len, D), BF16),
                             jax.ShapeDtypeStruct((s_len, D), BF16)], carry=carry)(dm, ah, aa, p, p)


def _local_step(x, ctx, tgt, mod, modc, nw1, nw2, lg, hw, qnw, knw, sinks,
                w_in, wts, dist=None):
    s_len = x.shape[0]
    tt = s_len + L
    ss1 = jnp.stack([modc, mod[0:2]])
    ss2 = mod[3:5][None]
    g1, g2 = mod[2:3], mod[5:6]
    hw4 = jnp.tile(hw, (1, 4))
    qnw8 = jnp.tile(qnw, (1, 8))
    knw2 = jnp.tile(knw, (1, 2))
    cos, sin = _rope_tables(s_len)
    bd512, bd128 = _blockdiag(ATW, HDIM), _blockdiag(128, HDIM)
    dupm = _dup_matrix()
    dup, dupt = jnp.asarray(dupm, BF16), jnp.asarray(dupm.T, F32)
    tmt = tt

    def four(b):
        return b.reshape(4, 2 * b.shape[1], b.shape[2])

    def halves(g):
        return g.reshape(4, 2, g.shape[1] // 2, g.shape[2])

    h = _mod1(ctx, x, nw1, ss1)
    if dist is None:
        bh4, ba4, w_o, g4, u4, dn4 = wts
        p = _mm_in(h, w_in, tmt)
        o0, st0 = _hgrn_fwd(p, lg, rev=False)
        o1, st1 = _hgrn_fwd(p, lg, rev=True)
    else:
        core, chip = dist
        half = wts[3].shape[1] // 2
        p, first = _mm_in(h, w_in, tmt, carry=_carry_join(_carry_gather(list(wts[0:3])),
                                                          _carry_gather([wts[3]], rows=[(0, half)])))
        (o0, st0), (g8,) = _hgrn_fwd(p, lg, rev=False, carry=_carry_gather([first[3]], rows=[(half, half)]))
        o1, st1 = _hgrn_fwd(p, lg, rev=True)
        bh4, ba4, w_o, g4 = four(first[0]), four(first[1]), four(first[2]).reshape(D, D), four(g8)
    y_hg = _readout(o0, o1, p, hw4)
    qr, k4, v4 = _qk_prep(p, cos, sin, qnw8, knw2, bd512, bd128, dup)
    if dist is None:
        y_at, lse = _attn_fwd(qr, k4, v4, sinks)
    else:
        (y_at, lse), (u8,) = _attn_fwd(qr, k4, v4, sinks, carry=_carry_gather([wts[4]]))
        u4 = four(u8)
    ah = _mm_cs(y_hg, bh4, name="mm_bh")
    aa = _mm_cs(y_at, ba4, name="mm_ba")
    mixed = _merge(ah, aa, p)
    ao, x1, h2 = _out_proj_mod2(mixed, w_o, x, g1, nw2, ss2)
    if dist is None:
        a4, b4, z4 = _ffn_up(h2, g4, u4)
    else:
        (a4, b4, z4), (dn8,) = _ffn_up(h2, g4, u4, carry=_carry_gather([wts[5]]))
        dn4 = four(dn8)
    sq, dx2, dyb, dg2 = _ffn_down_loss(z4, dn4, x1, g2, tgt)

    da4, db4 = _ffn_dz(dyb, dn4, a4, b4)
    g_dn = _ffn_gdn(z4, dyb)
    if dist is None:
        dh2 = _ffn_dh2(da4, db4, g4, u4)
    else:
        dn_units = [halves(g_dn)]
        dh2, dn_recv = _ffn_dh2(da4, db4, g4, u4, carry=_carry_pairx(dn_units))
        dn_pairs = _rs_pair_add(dn_units, dn_recv, core)
    if dist is None:
        g_g, g_u = _ffn_ggu(h2, da4, db4)
    else:
        (g_g, g_u), c_dn = _ffn_ggu(h2, da4, db4, carry=_carry_chipx(dn_pairs))
        red_dn = _rs_chip_add(dn_pairs, c_dn, core, chip)
    dx1, dattn, dss2, dnw2, dg1 = _mod2_bwd(x1, dh2, dx2, ao, nw2, ss2, g1)
    dm = _mm(dattn, w_o, name="mm_dm", mode="nt", tm=512, tn=D, tk=D)
    g_o = _mm(mixed, dattn, name="mm_go", mode="tn", out_dtype=BF16, tm=D, tn=D, tk=512)
    if dist is None:
        dp, dmh, dma = _merge_bwd(dm, ah, aa, p)
    else:
        gu_units = [halves(g_g), halves(g_u)]
        (dp, dmh, dma), gu_recv = _merge_bwd(dm, ah, aa, p, carry=_carry_pairx(gu_units))
        ffn_pairs = list(dn_pairs) + list(_rs_pair_add(gu_units, gu_recv, core))
    dy_hg = _mm_cs_nt(dmh, bh4, name="mm_dyh")
    dy_at = _mm_cs_nt(dma, ba4, name="mm_dya")
    g_bh = _mm_cs_tn(y_hg, dmh, D // 4, name="mm_gbh")
    g_ba = _mm_cs_tn(y_at, dma, D // 4, name="mm_gba")
    if dist is None:
        dp, do, dhw4 = _readout_bwd(o0, o1, p, hw4, dy_hg, dp)
        dq, dkw, dvw, dkc, dvc, dsk = _attn_bwd(qr, k4, v4, sinks, y_at, lse, dy_at)
        dp, dqnw8, dknw2 = _attn_post(p, cos, sin, qnw8, knw2, bd512, bd128, dupt, dq, dkw, dvw, dkc, dvc, dp)
    else:
        mix_units = [halves(g_bh), halves(g_ba), halves(g_o.reshape(4, D // 4, D))]
        (dp, do, dhw4), mix_recv = _readout_bwd(o0, o1, p, hw4, dy_hg, dp, carry=_carry_pairx(mix_units))
        mix_pairs = _rs_pair_add(mix_units, mix_recv, core)
        (dq, dkw, dvw, dkc, dvc, dsk), bwd = _attn_bwd(
            qr, k4, v4, sinks, y_at, lse, dy_at,
            carry=_carry_join(_carry_chipx(ffn_pairs[1:2]), _carry_sibx(red_dn)))
        red_g = _rs_chip_add(ffn_pairs[1:2], bwd[0:1], core, chip)
        (dp, dqnw8, dknw2), post = _attn_post(
            p, cos, sin, qnw8, knw2, bd512, bd128, dupt, dq, dkw, dvw, dkc, dvc, dp,
            carry=_carry_join(_carry_chipx(ffn_pairs[2:3]), _carry_sibx(red_g)))
        red_u = _rs_chip_add(ffn_pairs[2:3], post[0:1], core, chip)
    if dist is None:
        dp, dv0, dq0, dlg0 = _hgrn_bwd(p, lg, do, st0, dp, None, rev=False)
        dp, dlg1 = _hgrn_bwd(p, lg, do, st1, dp, (dv0, dq0), rev=True)
    else:
        (dp, dv0, dq0, dlg0), mid = _hgrn_bwd(p, lg, do, st0, dp, None, rev=False,
                                              carry=_carry_join(_carry_chipx(mix_pairs), _carry_sibx(red_u)))
        mix_reds = _rs_chip_add(mix_pairs, mid[0:3], core, chip)
        (dp, dlg1), mix_done = _hgrn_bwd(p, lg, do, st1, dp, (dv0, dq0), rev=True, carry=_carry_sibx(mix_reds))
        ffn_done = bwd[1:2] + post[1:2] + mid[3:4]
    dh = _mm_dh(dp, w_in, tmt)
    g_in = _mm_gin(dp, h, tmt)
    if dist is None:
        gx, dss1, dnw1 = _mod1_bwd(ctx, x, dh, dx1, nw1, ss1)
        rs = None
    else:
        in_units = [halves(g_in.reshape(4, NCOL // 4, D))]
        (gx, dss1, dnw1), in_recv = _mod1_bwd(ctx, x, dh, dx1, nw1, ss1, carry=_carry_pairx(in_units))
        rs = dict(ffn_done=ffn_done, mix_done=mix_done, in_units=in_units, in_recv=in_recv)

    dmod = jnp.concatenate([dss1[1], dg1, dss2, dg2], axis=0)
    dmodc = dss1[0]
    raw = (dss1, dg1, dss2, dg2, dnw1, dnw2, dhw4, dqnw8, dknw2, dsk, dlg0, dlg1)
    small = dict(raw=raw, dmod=dmod, dmodc=dmodc, dnw1=dnw1, dnw2=dnw2,
                 dhw=dhw4.reshape(4, HGD).sum(0, keepdims=True),
                 dqnw=dqnw8.reshape(8, HDIM).sum(0, keepdims=True),
                 dknw=dknw2.reshape(2, HDIM).sum(0, keepdims=True),
                 dsinks=dsk[:, 0], dlg=jnp.concatenate([dlg0, dlg1], axis=0))
    big = dict(w_in=g_in, w_bh=g_bh, w_ba=g_ba, w_o=g_o, w_g=g_g, w_u=g_u, w_dn=g_dn)
    return sq, gx, big, small, rs


def _place():
    x, y, c = lax.axis_index("x"), lax.axis_index("y"), lax.axis_index("c")
    return x, y, c


def _gather_blocks(x_refs, out_refs, send_sems, recv_sems, local_sems):
    n = len(out_refs)
    x, y, c = _place()
    me, sibling = (x, y, c), (x, y, 1 - c)
    chips = [(1 - x, y), (x, 1 - y), (1 - x, 1 - y)]

    def slot(u, px, py, pc):
        return out_refs[u].at[4 * px + 2 * py + pc]

    def copy(u, k, block, to, src=None):
        return pltpu.make_async_remote_copy(
            src_ref=slot(u, *block) if src is None else src, dst_ref=slot(u, *block),
            send_sem=send_sems.at[u, k], recv_sem=recv_sems.at[u, k], device_id=to, device_id_type=MESH)

    mines = []
    if x_refs is not None:
        mines = [pltpu.make_async_copy(x_refs[u], slot(u, *me), local_sems.at[u]) for u in range(n)]
    for cp in mines:
        cp.start()
    first = []
    for u in range(n):
        src = None if x_refs is None else x_refs[u]
        first.append(copy(u, 0, me, sibling, src=src))
        first += [copy(u, 1 + j, me, (*chip, c), src=src) for j, chip in enumerate(chips)]
    for cp in first:
        cp.start()
    passed = []
    for j, chip in enumerate(chips):
        for u in range(n):
            copy(u, 1 + j, (*chip, c), me).wait_recv()
            fwd = copy(u, 4 + j, (*chip, c), sibling)
            fwd.start()
            passed.append(fwd)
    for u in range(n):
        copy(u, 0, sibling, me).wait_recv()
    for j, chip in enumerate(chips):
        for u in range(n):
            copy(u, 4 + j, (*chip, 1 - c), me).wait_recv()
    for cp in first + passed:
        cp.wait_send()
    for cp in mines:
        cp.wait()


def _gather_sems(n):
    return [pltpu.SemaphoreType.DMA((n, 7)), pltpu.SemaphoreType.DMA((n, 7)), pltpu.SemaphoreType.DMA((n,))]


def _allgather(blks, *, name, in_vmem):
    n = len(blks)
    space = pltpu.VMEM if in_vmem else pl.ANY

    def body(*refs):
        _gather_blocks(refs[:n], refs[n:2 * n], *refs[2 * n:])

    return pl.pallas_call(
        body, name=name, out_shape=[jax.ShapeDtypeStruct((8,) + b.shape, b.dtype) for b in blks],
        in_specs=[pl.BlockSpec(memory_space=space)] * n, out_specs=[pl.BlockSpec(memory_space=space)] * n,
        scratch_shapes=_gather_sems(n))(*blks)


def _cast_place(ws, c, dev):
    n = len(ws)

    def body(s_ref, *refs):
        for u in range(n):
            refs[n + u][0] = refs[u][...].astype(BF16)

    in_specs, out_specs, out_shape = [], [], []
    for w in ws:
        q, cols = w.shape[0] // 4, w.shape[1]
        in_specs.append(pl.BlockSpec((q, cols), lambda i, s: (2 * s[0] + i, 0)))
        out_specs.append(pl.BlockSpec((1, q, cols), lambda i, s: (s[1], i, 0)))
        out_shape.append(jax.ShapeDtypeStruct((8, 2 * q, cols), BF16))
    return pl.pallas_call(
        body, name="cast_place",
        grid_spec=pltpu.PrefetchScalarGridSpec(num_scalar_prefetch=1, grid=(2,), in_specs=in_specs,
                                               out_specs=out_specs),
        out_shape=_out_hbm(out_shape),
        compiler_params=pltpu.CompilerParams(vmem_limit_bytes=48 << 20))(jnp.stack([c, dev]), *_in_hbm(ws))


def _gather_phases(out_refs, send_sems, recv_sems, rows=None):
    n = len(out_refs)
    x, y, c = _place()
    me, sibling = (x, y, c), (x, y, 1 - c)
    chips = [(1 - x, y), (x, 1 - y), (1 - x, 1 - y)]

    def copy(u, k, block, to):
        px, py, pc = block
        ref = out_refs[u].at[4 * px + 2 * py + pc]
        if rows is not None and rows[u] is not None:
            ref = ref.at[pl.ds(rows[u][0], rows[u][1])]
        return pltpu.make_async_remote_copy(src_ref=ref, dst_ref=ref, send_sem=send_sems.at[u, k],
                                            recv_sem=recv_sems.at[u, k], device_id=to, device_id_type=MESH)

    def start():
        for u in range(n):
            copy(u, 0, me, sibling).start()
            for j, chip in enumerate(chips):
                copy(u, 1 + j, me, (*chip, c)).start()

    def mid():
        for j, chip in enumerate(chips):
            for u in range(n):
                copy(u, 1 + j, (*chip, c), me).wait_recv()
                copy(u, 4 + j, (*chip, c), sibling).start()

    def end():
        for u in range(n):
            copy(u, 0, sibling, me).wait_recv()
        for j, chip in enumerate(chips):
            for u in range(n):
                copy(u, 4 + j, (*chip, 1 - c), me).wait_recv()
        for u in range(n):
            copy(u, 0, me, sibling).wait_send()
            for j, chip in enumerate(chips):
                copy(u, 1 + j, me, (*chip, c)).wait_send()
                copy(u, 4 + j, (*chip, c), sibling).wait_send()

    return start, mid, end


def _carry_gather(bufs, rows=None):
    n = len(bufs)
    return _Carry(bufs, [jax.ShapeDtypeStruct(b.shape, b.dtype) for b in bufs], {u: u for u in range(n)},
                  [pltpu.SemaphoreType.DMA((n, 7)), pltpu.SemaphoreType.DMA((n, 7))],
                  lambda ins, outs, sems: _gather_phases(outs, *sems, rows=rows))


def _allgather_inplace(bufs, *, name):
    n = len(bufs)

    def body(*refs):
        for phase in _gather_phases(refs[n:2 * n], *refs[2 * n:]):
            phase()

    return pl.pallas_call(
        body, name=name, out_shape=[jax.ShapeDtypeStruct(b.shape, b.dtype) for b in bufs],
        in_specs=[ANY] * n, out_specs=[ANY] * n, input_output_aliases={u: u for u in range(n)},
        scratch_shapes=[pltpu.SemaphoreType.DMA((n, 7)), pltpu.SemaphoreType.DMA((n, 7))])(*bufs)


def _ag_small(raw):
    def body(dss1, dg1, dss2, dg2, dnw1, dnw2, dhw4, dqnw8, dknw2, dsk, dlg0, dlg1,
             out_ref, tot_ref, blk, send_sems, recv_sems, local_sems):
        blk[...] = jnp.zeros_like(blk)
        blk[0:2, :] = dss1[1]
        blk[2:3, :] = dg1[...]
        blk[3:5, :] = dss2[...]
        blk[5:6, :] = dg2[...]
        blk[6:8, :] = dss1[0]
        blk[8:9, :] = dnw1[...]
        blk[9:10, :] = dnw2[...]
        blk[10:11, 0:HGW] = dhw4[...]
        blk[10:11, HGW:D] = dqnw8[...]
        blk[11:12, 0:128] = dknw2[...]
        blk[12:14, 0:HGW] = dlg0[0]
        blk[14:16, 0:HGW] = dlg1[0]
        blk[16:24, 0:128] = dsk[...]
        _gather_blocks([blk], [out_ref], send_sems, recv_sems, local_sems)
        acc = out_ref[0]
        for i in range(1, 8):
            acc = acc + out_ref[i]
        tot_ref[...] = acc

    vm = pl.BlockSpec(memory_space=pltpu.VMEM)
    return pl.pallas_call(
        body, name="ag_small",
        out_shape=[jax.ShapeDtypeStruct((8, 24, D), F32), jax.ShapeDtypeStruct((24, D), F32)],
        in_specs=[vm] * 12, out_specs=[vm, vm],
        scratch_shapes=[pltpu.VMEM((24, D), F32)] + _gather_sems(1))(*raw)


def _rs_pair_exchange(units):
    n = len(units)

    def body(*refs):
        start, _, end = _pairx_phases(refs[:n], refs[n:2 * n], *refs[2 * n:])
        start()
        end()

    return pl.pallas_call(
        body, name="rs_pair_exchange", out_shape=_pairx_shapes(units),
        in_specs=[ANY] * n, out_specs=[ANY] * n,
        scratch_shapes=[pltpu.SemaphoreType.DMA((n, 4)), pltpu.SemaphoreType.DMA((n, 4))])(*units)


def _pairx_shapes(units):
    return [jax.ShapeDtypeStruct((4,) + g.shape[2:], g.dtype) for g in units]


def _pairx_phases(g_refs, r_refs, send_sems, recv_sems):
    n = len(g_refs)
    x, y, c = _place()
    cps = [pltpu.make_async_remote_copy(
        src_ref=g_refs[u].at[j, 1 - c], dst_ref=r_refs[u].at[j], send_sem=send_sems.at[u, j],
        recv_sem=recv_sems.at[u, j], device_id=(x, y, 1 - c), device_id_type=MESH)
        for u in range(n) for j in range(4)]

    def start():
        for cp in cps:
            cp.start()

    def end():
        for cp in cps:
            cp.wait()

    return start, None, end


def _carry_pairx(units):
    n = len(units)
    return _Carry(units, _pairx_shapes(units), {},
                  [pltpu.SemaphoreType.DMA((n, 4)), pltpu.SemaphoreType.DMA((n, 4))],
                  lambda ins, outs, sems: _pairx_phases(ins, outs, *sems))


def _rs_pair_add(units, recvs, c):
    n = len(units)

    def body(c_ref, *refs):
        for u in range(n):
            refs[2 * n + u][...] = (refs[u][0].astype(F32) + refs[n + u][...].astype(F32)).astype(BF16)

    in_specs, out_specs, out_shape = [], [], []
    for g in units:
        h, w = g.shape[2] // 2, g.shape[3]
        in_specs.append(pl.BlockSpec((1, 1, h, w), lambda j, i, cr: (j, cr[0], i, 0)))
    for g in units:
        h, w = g.shape[2] // 2, g.shape[3]
        in_specs.append(pl.BlockSpec((1, h, w), lambda j, i, cr: (j, i, 0)))
        out_specs.append(pl.BlockSpec((1, h, w), lambda j, i, cr: (j, i, 0)))
        out_shape.append(jax.ShapeDtypeStruct((4, 2 * h, w), BF16))
    return pl.pallas_call(
        body, name="rs_pair_add",
        grid_spec=pltpu.PrefetchScalarGridSpec(num_scalar_prefetch=1, grid=(4, 2), in_specs=in_specs,
                                               out_specs=out_specs),
        out_shape=_out_hbm(out_shape),
        compiler_params=pltpu.CompilerParams(vmem_limit_bytes=48 << 20))(
            c.reshape(1), *_in_hbm(list(units) + list(recvs)))


def _rs_chip_exchange(pairs):
    n = len(pairs)

    def body(*refs):
        start, _, end = _chipx_phases(refs[:n], refs[n:2 * n], *refs[2 * n:])
        start()
        end()

    return pl.pallas_call(
        body, name="rs_chip_exchange", out_shape=[jax.ShapeDtypeStruct(p.shape, p.dtype) for p in pairs],
        in_specs=[ANY] * n, out_specs=[ANY] * n,
        scratch_shapes=[pltpu.SemaphoreType.DMA((n, 3)), pltpu.SemaphoreType.DMA((n, 3))])(*pairs)


def _chipx_phases(p_refs, r_refs, send_sems, recv_sems):
    n = len(p_refs)
    x, y, c = _place()
    k = 2 * x + y
    sends = []
    for d in range(1, 4):
        j = (k + d) % 4
        for u in range(n):
            sends.append(pltpu.make_async_remote_copy(
                src_ref=p_refs[u].at[j], dst_ref=r_refs[u].at[k], send_sem=send_sems.at[u, d - 1],
                recv_sem=recv_sems.at[u, d - 1], device_id=(j // 2, j % 2, c), device_id_type=MESH))

    def start():
        for cp in sends:
            cp.start()

    def end():
        for d in range(1, 4):
            src = (k + 4 - d) % 4
            for u in range(n):
                pltpu.make_async_remote_copy(
                    src_ref=p_refs[u].at[src], dst_ref=r_refs[u].at[src], send_sem=send_sems.at[u, d - 1],
                    recv_sem=recv_sems.at[u, d - 1], device_id=(x, y, c), device_id_type=MESH).wait_recv()
        for cp in sends:
            cp.wait_send()

    return start, None, end


def _carry_chipx(pairs):
    n = len(pairs)
    return _Carry(pairs, [jax.ShapeDtypeStruct(p.shape, p.dtype) for p in pairs], {},
                  [pltpu.SemaphoreType.DMA((n, 3)), pltpu.SemaphoreType.DMA((n, 3))],
                  lambda ins, outs, sems: _chipx_phases(ins, outs, *sems))


def _rs_chip_add(pairs, contribs, c, chip):
    n = len(pairs)

    def body(s_ref, *refs):
        for u in range(n):
            a, b, c_, d = refs[4 * u:4 * u + 4]
            refs[4 * n + u][0] = ((a[0].astype(F32) + b[0].astype(F32)) + c_[0].astype(F32)) + d[0].astype(F32)

    in_specs, out_specs, out_shape, args = [], [], [], []
    for p, r in zip(pairs, contribs):
        h, w = p.shape[1] // 2, p.shape[2]
        in_specs += [pl.BlockSpec((1, h, w), functools.partial(lambda d, i, s: ((s[1] + d) % 4, i, 0), d))
                     for d in range(4)]
        args += [p, r, r, r]
        out_specs.append(pl.BlockSpec((1, h, w), lambda i, s: (s[0], i, 0)))
        out_shape.append(jax.ShapeDtypeStruct((2, 2 * h, w), F32))
    return pl.pallas_call(
        body, name="rs_chip_add",
        grid_spec=pltpu.PrefetchScalarGridSpec(num_scalar_prefetch=1, grid=(2,), in_specs=in_specs,
                                               out_specs=out_specs),
        out_shape=_out_hbm(out_shape),
        compiler_params=pltpu.CompilerParams(vmem_limit_bytes=48 << 20))(jnp.stack([c, chip]), *_in_hbm(args))


def _rs_sibling_gather(reds):
    n = len(reds)

    def body(*refs):
        start, _, end = _sibx_phases(refs[n:2 * n], *refs[2 * n:])
        start()
        end()

    return pl.pallas_call(
        body, name="rs_sibling_gather", out_shape=[jax.ShapeDtypeStruct(r.shape, r.dtype) for r in reds],
        in_specs=[ANY] * n, out_specs=[ANY] * n, input_output_aliases={u: u for u in range(n)},
        scratch_shapes=[pltpu.SemaphoreType.DMA((n,))] * 2)(*reds)


def _sibx_phases(o_refs, send_sems, recv_sems):
    n = len(o_refs)
    x, y, c = _place()
    cps = [pltpu.make_async_remote_copy(
        src_ref=o_refs[u].at[c], dst_ref=o_refs[u].at[c], send_sem=send_sems.at[u], recv_sem=recv_sems.at[u],
        device_id=(x, y, 1 - c), device_id_type=MESH) for u in range(n)]

    def start():
        for cp in cps:
            cp.start()

    def end():
        for u in range(n):
            cps[u].wait_send()
            pltpu.make_async_remote_copy(
                src_ref=o_refs[u].at[1 - c], dst_ref=o_refs[u].at[1 - c], send_sem=send_sems.at[u],
                recv_sem=recv_sems.at[u], device_id=(x, y, 1 - c), device_id_type=MESH).wait_recv()

    return start, None, end


def _carry_sibx(reds):
    n = len(reds)
    return _Carry(reds, [jax.ShapeDtypeStruct(r.shape, r.dtype) for r in reds], {u: u for u in range(n)},
                  [pltpu.SemaphoreType.DMA((n,))] * 2, lambda ins, outs, sems: _sibx_phases(outs, *sems))


def _prologue(blk, c_ctx, w, b, in8):
    n = w.shape[1]

    def body(blk_ref, cctx_ref, w_ref, b_ref, _in_in, g0_ref, c16_ref, g1_ref, in_ref, mod_s,
             s1, r1, l1, s2, r2, l2, s3, r3):
        start, mid, end = _gather_phases([in_ref], s3, r3)
        start()
        _gather_blocks([blk_ref], [g0_ref], s1, r1, l1)
        c16 = jnp.concatenate([g0_ref[i, 0:1, :] for i in range(8)] + [cctx_ref[...], jnp.zeros((7, D), F32)],
                              axis=0)
        c16_ref[...] = c16
        mod_s[...] = _dot(c16 * _sig(c16), w_ref[...], prec=HI) + b_ref[...]
        _gather_blocks([mod_s], [g1_ref], s2, r2, l2)
        mid()
        end()

    vm = pl.BlockSpec(memory_space=pltpu.VMEM)
    return pl.pallas_call(
        body, name="prologue",
        out_shape=[jax.ShapeDtypeStruct((8, 8, D), F32), jax.ShapeDtypeStruct((16, D), F32),
                   jax.ShapeDtypeStruct((8, 16, n), F32), jax.ShapeDtypeStruct(in8.shape, in8.dtype)],
        in_specs=[vm, vm, vm, vm, ANY], out_specs=[vm, vm, vm, ANY], input_output_aliases={4: 3},
        scratch_shapes=[pltpu.VMEM((16, n), F32)] + _gather_sems(1) + _gather_sems(1)
        + [pltpu.SemaphoreType.DMA((1, 7)), pltpu.SemaphoreType.DMA((1, 7))],
        compiler_params=pltpu.CompilerParams(vmem_limit_bytes=48 << 20))(blk, c_ctx, w, b, in8)


def _ada_bwd(c16, dmod16, w):
    n = w.shape[1]
    tn = 512

    def body(c_ref, d_ref, w_ref, gw_ref, gc_ref):
        j = pl.program_id(0)

        @pl.when(j == 0)
        def _():
            gc_ref[...] = jnp.zeros_like(gc_ref)

        cc = c_ref[...]
        dm = d_ref[...]
        gw_ref[...] = _dot(cc * _sig(cc), dm, TN, prec=HI)
        gc_ref[...] += _dot(dm, w_ref[...], NT, prec=HI)

    return _pcall(body, name="ada_bwd", grid=(n // tn,),
                  in_specs=[_full((16, D)), pl.BlockSpec((16, tn), lambda j: (0, j)),
                            pl.BlockSpec((D, tn), lambda j: (0, j))],
                  out_specs=[pl.BlockSpec((D, tn), lambda j: (0, j)), _full((16, D))],
                  out_shape=[jax.ShapeDtypeStruct((D, n), F32),
                             jax.ShapeDtypeStruct((16, D), F32)])(c16, dmod16, w)


def _adam_math(w, g, m, v):
    c1 = 1.0 - ADAM_B1 ** ADAM_STEP
    c2 = 1.0 - ADAM_B2 ** ADAM_STEP
    nm = ADAM_B1 * m + (1.0 - ADAM_B1) * g
    nv = ADAM_B2 * v + (1.0 - ADAM_B2) * (g * g)
    return -ADAM_LR * ((nm / c1) / (jnp.sqrt(nv / c2) + ADAM_EPS) + ADAM_WD * w), nm, nv


def _adamw_small(ws, gs, ms, vs):
    n = len(ws)

    def body(*refs):
        for u in range(n):
            d_, nm, nv = _adam_math(refs[u][...], refs[n + u][...], refs[2 * n + u][...], refs[3 * n + u][...])
            refs[4 * n + u][...] = d_
            refs[5 * n + u][...] = nm
            refs[6 * n + u][...] = nv

    specs = [_full(w.shape) for w in ws]
    shapes = [jax.ShapeDtypeStruct(w.shape, F32) for w in ws]
    out = _pcall(body, name="adamw_small", grid=(1,), in_specs=specs * 4, out_specs=specs * 3,
                 out_shape=shapes * 3)(*ws, *gs, *ms, *vs)
    return out[:n], out[n:2 * n], out[2 * n:]


def _cctx_grad(parts, c_ctx):
    def body(p_ref, c_ref, o_ref):
        acc = p_ref[0:1, :]
        for k in range(1, 4):
            acc = acc + p_ref[k:k + 1, :]
        cc = c_ref[...]
        s = _sig(cc)
        o_ref[...] = acc * (s * (1.0 + cc * (1.0 - s)))

    return _pcall(body, name="cctx_grad", grid=(1,), in_specs=[_full(parts.shape), _full((1, D))],
                  out_specs=_full((1, D)), out_shape=jax.ShapeDtypeStruct((1, D), F32))(parts, c_ctx)


ADAM_STEPS = 8


def _adamw_multi(ws, gs, ms, vs, *, name, carry=None):
    n = len(ws)

    def body(*refs):
        for u in range(n):
            refs[4 * n + u][...], refs[5 * n + u][...], refs[6 * n + u][...] = _adam_math(
                refs[u][...], refs[n + u][...], refs[2 * n + u][...], refs[3 * n + u][...])

    specs = [pl.BlockSpec((w.shape[0] // ADAM_STEPS, w.shape[1]), lambda i: (i, 0)) for w in ws]
    shapes = [jax.ShapeDtypeStruct(w.shape, F32) for w in ws]
    res = _pcall(body, name=name, grid=(ADAM_STEPS,), in_specs=specs * 4, out_specs=specs * 3,
                 out_shape=shapes * 3, carry=carry)(*ws, *gs, *ms, *vs)
    out, extra = res if carry is not None else (res, None)
    return (out[:n], out[n:2 * n], out[2 * n:]), extra


def kernel(x, c, ctx, c_ctx, w_ada, b_ada, norm_mix_w, norm_ffn_w, w_in, hgrn_lb_logits, hgrn_norm_w, q_norm_w, k_norm_w, attn_sinks, w_branch_hgrn, w_branch_attn, w_out, w_ffn_gate, w_ffn_up, w_ffn_down, loss_target, m_c_ctx, m_w_ada, m_b_ada, m_norm_mix_w, m_norm_ffn_w, m_w_in, m_hgrn_lb_logits, m_hgrn_norm_w, m_q_norm_w, m_k_norm_w, m_attn_sinks, m_w_branch_hgrn, m_w_branch_attn, m_w_out, m_w_ffn_gate, m_w_ffn_up, m_w_ffn_down, v_c_ctx, v_w_ada, v_b_ada, v_norm_mix_w, v_norm_ffn_w, v_w_in, v_hgrn_lb_logits, v_hgrn_norm_w, v_q_norm_w, v_k_norm_w, v_attn_sinks, v_w_branch_hgrn, v_w_branch_attn, v_w_out, v_w_ffn_gate, v_w_ffn_up, v_w_ffn_down):
    xi, yi, ci = _place()
    chip = 2 * xi + yi
    dev = 2 * chip + ci
    s_len = x.shape[1]

    shards = [w_in[0].T, w_branch_hgrn[0], w_branch_attn[0], w_out[0], w_ffn_gate[0].T, w_ffn_up[0].T,
              w_ffn_down[0]]
    bufs = _cast_place(shards, ci, dev)

    lbrow = jnp.pad(hgrn_lb_logits.reshape(1, 512), ((0, 0), (0, D - 512)))
    blk = jnp.concatenate([c, lbrow, jnp.zeros((6, D), F32)], axis=0)
    nada = w_ada.shape[2]
    b_sh = lax.dynamic_slice(b_ada, (0, chip * nada), (1, nada))
    g0, c16, g1, in8 = _prologue(blk, c_ctx[None], w_ada[0], b_sh, bufs[0])
    lg = g0[0::2, 1, :512].reshape(4, 2, 2, 128).transpose(1, 2, 0, 3).reshape(2, 2, HGW)
    modall = g1[0::2].transpose(1, 0, 2).reshape(16, 4 * nada)
    mod = lax.dynamic_slice(modall, (dev, 0), (1, 6 * D)).reshape(6, D)
    modc = modall[8].reshape(6, D)[:2]

    sq, gx, _, small, rs = _local_step(
        x[0], ctx[0], loss_target[0], mod, modc, norm_mix_w, norm_ffn_w, lg, hgrn_norm_w, q_norm_w,
        k_norm_w, attn_sinks[0], in8.reshape(NCOL, D), bufs[1:], dist=(ci, chip))
    loss = lax.psum(0.5 * jnp.sum(sq) / D, ("x", "y", "c"))

    def whole(r):
        return r.reshape(2 * r.shape[1], r.shape[2])

    g_dn, g_g, g_u = [whole(r) for r in rs["ffn_done"]]
    g_bh, g_ba, g_o = [whole(r) for r in rs["mix_done"]]
    in_pairs = _rs_pair_add(rs["in_units"], rs["in_recv"], ci)

    g2, tot = _ag_small(small["raw"])
    dmodc_tot = jnp.pad(tot[6:8].reshape(1, 2 * D), ((0, 0), (0, 4 * D)))
    g_b_ada = tot[0:6].reshape(1, 6 * D) + dmodc_tot
    dmod16 = jnp.concatenate([g2[:, 0:6].reshape(8, 6 * D), dmodc_tot, jnp.zeros((7, 6 * D), F32)], axis=0)
    g_w_ada, gc_part = _ada_bwd(c16, lax.dynamic_slice(dmod16, (0, chip * nada), (16, nada)), w_ada[0])
    g3, = _allgather([gc_part[8:16]], name="ag_cctx", in_vmem=True)
    g_c_ctx = _cctx_grad(g3[0::2, 0], c_ctx[None])[0]
    g_nw1 = tot[8:9]
    g_nw2 = tot[9:10]
    g_hw = tot[10, :HGW].reshape(4, HGD).sum(0, keepdims=True)
    g_qnw = tot[10, HGW:].reshape(8, HDIM).sum(0, keepdims=True)
    g_knw = tot[11, :128].reshape(2, HDIM).sum(0, keepdims=True)
    g_sinks = tot[16:24, 0][None]
    g_lg = lax.dynamic_slice(tot[12:16, :HGW].reshape(2, 2, HGW), (0, 0, chip * 128), (2, 2, 128))

    names = ["c_ctx", "w_ada", "b_ada", "norm_mix_w", "norm_ffn_w", "w_in", "hgrn_lb_logits", "hgrn_norm_w",
             "q_norm_w", "k_norm_w", "attn_sinks", "w_branch_hgrn", "w_branch_attn", "w_out", "w_ffn_gate",
             "w_ffn_up", "w_ffn_down"]
    ws = dict(zip(names, [c_ctx, w_ada, b_ada, norm_mix_w, norm_ffn_w, w_in, hgrn_lb_logits, hgrn_norm_w,
                          q_norm_w, k_norm_w, attn_sinks, w_branch_hgrn, w_branch_attn, w_out, w_ffn_gate,
                          w_ffn_up, w_ffn_down]))
    ms = dict(zip(names, [m_c_ctx, m_w_ada, m_b_ada, m_norm_mix_w, m_norm_ffn_w, m_w_in, m_hgrn_lb_logits,
                          m_hgrn_norm_w, m_q_norm_w, m_k_norm_w, m_attn_sinks, m_w_branch_hgrn,
                          m_w_branch_attn, m_w_out, m_w_ffn_gate, m_w_ffn_up, m_w_ffn_down]))
    vs = dict(zip(names, [v_c_ctx, v_w_ada, v_b_ada, v_norm_mix_w, v_norm_ffn_w, v_w_in, v_hgrn_lb_logits,
                          v_hgrn_norm_w, v_q_norm_w, v_k_norm_w, v_attn_sinks, v_w_branch_hgrn,
                          v_w_branch_attn, v_w_out, v_w_ffn_gate, v_w_ffn_up, v_w_ffn_down]))
    transposed = ("w_in", "w_ffn_gate", "w_ffn_up")

    def view(a, n):
        return a[0].T if n in transposed else a[0]

    def unview(a, n):
        return a.T[None] if n in transposed else a[None]

    delta, new_m, new_v, grads = {}, {}, {}, {}

    def big_adamw(group, gs, name, carry=None):
        (d_, m_, v_), extra = _adamw_multi([view(ws[n], n) for n in group], gs, [view(ms[n], n) for n in group],
                                           [view(vs[n], n) for n in group], name=name, carry=carry)
        for i, n in enumerate(group):
            grads[n], delta[n], new_m[n], new_v[n] = (unview(gs[i], n), unview(d_[i], n), unview(m_[i], n),
                                                      unview(v_[i], n))
        return extra

    in_contribs = big_adamw(["w_ffn_down", "w_ffn_gate", "w_ffn_up", "w_out", "w_branch_hgrn", "w_branch_attn"],
                            [g_dn, g_g, g_u, g_o, g_bh, g_ba], "adamw_first", carry=_carry_chipx(in_pairs))
    in_reds = _rs_chip_add(in_pairs, in_contribs, ci, chip)
    g_in, = [whole(r) for r in _rs_sibling_gather(in_reds)]
    big_adamw(["w_in", "w_ada"], [g_in, g_w_ada], "adamw_second")
    grads.update(c_ctx=g_c_ctx, b_ada=g_b_ada, norm_mix_w=g_nw1, norm_ffn_w=g_nw2, hgrn_lb_logits=g_lg,
                 hgrn_norm_w=g_hw, q_norm_w=g_qnw, k_norm_w=g_knw, attn_sinks=g_sinks)
    small_names = [n for n in names if n not in delta]

    def two_d(a):
        return a.reshape(1, -1) if a.ndim == 1 else a

    sd, sm_, sv = _adamw_small(*[[two_d(d[n]) for n in small_names] for d in (ws, grads, ms, vs)])
    for i, n in enumerate(small_names):
        for dst, src in ((delta, sd), (new_m, sm_), (new_v, sv)):
            dst[n] = src[i].reshape(ws[n].shape)
    return (loss, gx[None], *[grads[n] for n in names], *[delta[n] for n in names],
            *[new_m[n] for n in names], *[new_v[n] for n in names])
```

```python
import functools

import numpy as np
import jax
import jax.numpy as jnp
from jax import lax
from jax.experimental import pallas as pl
from jax.experimental.pallas import tpu as pltpu

F32 = jnp.float32
BF16 = jnp.bfloat16
HI = lax.Precision.HIGHEST
MESH = pl.DeviceIdType.MESH

D = 1024
L = 256
TM = 256
HGW = 512
HGD = 128
CH = 32
ATW = 512
HDIM = 64
BLK = 128
GRID_W = 64
DFF = 2816
NCOL = 5376
EPS = 1e-6
ROPE_THETA = 10000.0

C_FB, C_INP, C_QHG, C_FF = 0, 1, 2, 3
C_GATES = 1
C_GHG, C_QRAW = 8, 9
C_KV = 20
C_QKV = 6

ADAM_LR, ADAM_B1, ADAM_B2, ADAM_EPS, ADAM_WD, ADAM_STEP = 0.001, 0.9, 0.999, 1e-08, 0.01, 10

NN = (((1,), (0,)), ((), ()))
NT = (((1,), (1,)), ((), ()))
TN = (((0,), (0,)), ((), ()))


def _dot(a, b, dims=NN, prec=None):
    return lax.dot_general(a, b, dims, precision=prec, preferred_element_type=F32)


def _bdot(a, b, dims=NN):
    return _dot(a.astype(BF16), b.astype(BF16), dims)


def _sig(x):
    return 1.0 / (1.0 + jnp.exp(-x))


class _Carry:
    def __init__(self, ins, outs, aliases, scratch, phases):
        self.ins, self.outs, self.aliases, self.scratch, self.phases = ins, outs, aliases, scratch, phases


def _in_hbm(args):
    return [pltpu.with_memory_space_constraint(a, pltpu.HBM) for a in args]


def _out_hbm(shapes):
    if isinstance(shapes, (list, tuple)):
        return [pltpu.HBM(s.shape, s.dtype) for s in shapes]
    return pltpu.HBM(shapes.shape, shapes.dtype)


def _carry_join(a, b):
    na_in, na_out, na_sc = len(a.ins), len(a.outs), len(a.scratch)
    aliases = dict(a.aliases)
    aliases.update({na_in + i: na_out + o for i, o in b.aliases.items()})

    def phases(ins, outs, sems):
        pa = a.phases(ins[:na_in], outs[:na_out], sems[:na_sc])
        pb = b.phases(ins[na_in:], outs[na_out:], sems[na_sc:])

        def both(fa, fb):
            if fa is None and fb is None:
                return None

            def run():
                for fn in (fa, fb):
                    if fn is not None:
                        fn()
            return run

        return tuple(both(fa, fb) for fa, fb in zip(pa, pb))

    return _Carry(list(a.ins) + list(b.ins), list(a.outs) + list(b.outs), aliases,
                  list(a.scratch) + list(b.scratch), phases)


def _pcall(body, *, name, grid, in_specs, out_specs, out_shape, scratch=(), aliases=None, vmem_mb=48,
           carry=None):
    params = pltpu.CompilerParams(dimension_semantics=("arbitrary",) * len(grid),
                                  vmem_limit_bytes=vmem_mb << 20)
    if carry is None:
        plain = pl.pallas_call(
            body, name=name, grid=grid, in_specs=in_specs, out_specs=out_specs, out_shape=_out_hbm(out_shape),
            scratch_shapes=list(scratch), input_output_aliases=aliases or {}, compiler_params=params)
        return lambda *args: plain(*_in_hbm(args))
    single = not isinstance(out_shape, (list, tuple))
    out_specs_l = [out_specs] if single else list(out_specs)
    out_shape_l = [out_shape] if single else list(out_shape)
    n_in, n_out, n_sc = len(in_specs), len(out_shape_l), len(scratch)
    k_in, k_out = len(carry.ins), len(carry.outs)
    nsteps = int(np.prod(grid))
    assert nsteps >= 3

    def wrapped(*refs):
        ins, cins = refs[:n_in], refs[n_in:n_in + k_in]
        o0 = n_in + k_in
        outs, couts = refs[o0:o0 + n_out], refs[o0 + n_out:o0 + n_out + k_out]
        s0 = o0 + n_out + k_out
        sc, csc = refs[s0:s0 + n_sc], refs[s0 + n_sc:]
        step = pl.program_id(0)
        for ax in range(1, len(grid)):
            step = step * grid[ax] + pl.program_id(ax)
        start, mid, end = carry.phases(cins, couts, csc)
        pl.when(step == 0)(start)
        body(*ins, *outs, *sc)
        if mid is not None:
            pl.when(step == nsteps - 2)(mid)
        pl.when(step == nsteps - 1)(end)

    all_aliases = dict(aliases or {})
    all_aliases.update({n_in + i: n_out + o for i, o in carry.aliases.items()})
    call = pl.pallas_call(
        wrapped, name=name, grid=grid, in_specs=list(in_specs) + [ANY] * k_in,
        out_specs=out_specs_l + [ANY] * k_out, out_shape=_out_hbm(out_shape_l + list(carry.outs)),
        scratch_shapes=list(scratch) + list(carry.scratch), input_output_aliases=all_aliases,
        compiler_params=params)

    def run(*args):
        res = call(*_in_hbm(args), *carry.ins)
        core = res[:n_out]
        return (core[0] if single else list(core)), list(res[n_out:])

    return run


def _full(shape):
    nd = len(shape)
    return pl.BlockSpec(shape, lambda *_: (0,) * nd)


ANY = pl.BlockSpec(memory_space=pl.ANY)


def _mm(a, b, *, name, mode="nn", out_dtype=F32, tm, tn, tk):
    if mode == "nn":
        (m, k), (k2, n) = a.shape, b.shape
    elif mode == "nt":
        (m, k), (n, k2) = a.shape, b.shape
    else:
        (k, m), (k2, n) = a.shape, b.shape
    assert k == k2 and m % tm == 0 and n % tn == 0 and k % tk == 0, (name, a.shape, b.shape)
    nk = k // tk
    dims = {"nn": NN, "nt": NT, "tn": TN}[mode]

    def body(a_ref, b_ref, o_ref, acc):
        kk = pl.program_id(2)

        @pl.when(kk == 0)
        def _():
            acc[...] = jnp.zeros_like(acc)

        acc[...] += _bdot(a_ref[...], b_ref[...], dims)

        @pl.when(kk == nk - 1)
        def _():
            o_ref[...] = acc[...].astype(out_dtype)

    a_spec = (pl.BlockSpec((tk, tm), lambda i, j, kk: (kk, i)) if mode == "tn"
              else pl.BlockSpec((tm, tk), lambda i, j, kk: (i, kk)))
    b_spec = (pl.BlockSpec((tn, tk), lambda i, j, kk: (j, kk)) if mode == "nt"
              else pl.BlockSpec((tk, tn), lambda i, j, kk: (kk, j)))
    return _pcall(body, name=name, grid=(m // tm, n // tn, nk), in_specs=[a_spec, b_spec],
                  out_specs=pl.BlockSpec((tm, tn), lambda i, j, kk: (i, j)),
                  out_shape=jax.ShapeDtypeStruct((m, n), out_dtype),
                  scratch=[pltpu.VMEM((tm, tn), F32)])(a, b)


NT_IN = NCOL // 256


def _src_block(j):
    return j + jnp.where(j < 4, 2, jnp.where(j < 6, 3, jnp.where(j < 8, -6, jnp.where(
        j < 16, 5, jnp.where(j < 20, -7, -14)))))


def _mm_in(h, wt, tm, carry=None):
    tt = h.shape[0]

    def body(h_ref, w_ref, o_ref):
        o_ref[...] = _bdot(h_ref[...], w_ref[...], NT)

    return _pcall(body, name="mm_in", grid=(tt // tm, NT_IN),
                  in_specs=[pl.BlockSpec((tm, D), lambda i, j: (i, 0)),
                            pl.BlockSpec((256, D), lambda i, j: (_src_block(j), 0))],
                  out_specs=pl.BlockSpec((tm, 256), lambda i, j: (i, j)),
                  out_shape=jax.ShapeDtypeStruct((tt, NCOL), F32), carry=carry)(h, wt)


def _mm_dh(dp, wt, tm, carry=None):
    tt = dp.shape[0]
    per, ng = 3, NT_IN // 3

    def body(d_ref, w0, w1, w2, o_ref, acc):
        kk = pl.program_id(1)

        @pl.when(kk == 0)
        def _():
            acc[...] = jnp.zeros_like(acc)

        acc[...] += (_bdot(d_ref[:, 0:256], w0[...]) + _bdot(d_ref[:, 256:512], w1[...])
                     + _bdot(d_ref[:, 512:768], w2[...]))

        @pl.when(kk == ng - 1)
        def _():
            o_ref[...] = acc[...]

    wspecs = [pl.BlockSpec((256, D), functools.partial(lambda t, i, kk: (_src_block(per * kk + t), 0), t))
              for t in range(per)]
    return _pcall(body, name="mm_dh", grid=(tt // tm, ng),
                  in_specs=[pl.BlockSpec((tm, per * 256), lambda i, kk: (i, kk))] + wspecs,
                  out_specs=pl.BlockSpec((tm, D), lambda i, kk: (i, 0)),
                  out_shape=jax.ShapeDtypeStruct((tt, D), F32), scratch=[pltpu.VMEM((tm, D), F32)],
                  carry=carry)(dp, wt, wt, wt)


def _mm_gin(dp, h, tk):
    tt = dp.shape[0]
    nk = tt // tk

    def body(d_ref, h_ref, o_ref, acc):
        kk = pl.program_id(1)

        @pl.when(kk == 0)
        def _():
            acc[...] = jnp.zeros_like(acc)

        acc[...] += _bdot(d_ref[...], h_ref[...], TN)

        @pl.when(kk == nk - 1)
        def _():
            o_ref[...] = acc[...].astype(BF16)

    return _pcall(body, name="mm_gin", grid=(NT_IN, nk),
                  in_specs=[pl.BlockSpec((tk, 256), lambda j, kk: (kk, j)),
                            pl.BlockSpec((tk, D), lambda j, kk: (kk, 0))],
                  out_specs=pl.BlockSpec((256, D), lambda j, kk: (_src_block(j), 0)),
                  out_shape=jax.ShapeDtypeStruct((NCOL, D), BF16), scratch=[pltpu.VMEM((256, D), F32)])(dp, h)


def _tok_specs():
    assert L == TM
    return [_full((TM, D)), pl.BlockSpec((TM, D), lambda i: (jnp.maximum(i - 1, 0), 0))]


def _mod1(ctx, x, nw, ss):
    rows = L + x.shape[0]

    def body(c_ref, x_ref, nw_ref, ss_ref, h_ref):
        t = jnp.where(pl.program_id(0) == 0, c_ref[...], x_ref[...])
        r = lax.rsqrt(jnp.mean(t * t, axis=-1, keepdims=True) + EPS)
        s = ss_ref[0]
        h_ref[...] = ((t * r * nw_ref[...]) * (1.0 + s[1:2]) + s[0:1]).astype(BF16)

    return _pcall(body, name="mod1", grid=(rows // TM,),
                  in_specs=_tok_specs() + [_full((1, D)),
                                           pl.BlockSpec((1, 2, D), lambda i: (jnp.minimum(i, 1), 0, 0))],
                  out_specs=pl.BlockSpec((TM, D), lambda i: (i, 0)),
                  out_shape=jax.ShapeDtypeStruct((rows, D), BF16))(ctx, x, nw, ss)


def _norm_bwd_rows(x, dh, nw, scale):
    r = lax.rsqrt(jnp.mean(x * x, axis=-1, keepdims=True) + EPS)
    xh = x * r
    dxh = dh * ((1.0 + scale) * nw)
    dx = r * (dxh - xh * jnp.mean(dxh * xh, axis=-1, keepdims=True))
    return dx, xh


def _out_proj_mod2(mixed, w_o, x, g1, nw2, ss2):
    s_len = x.shape[0]
    tm = 512

    def body(m_ref, w_ref, x_ref, g_ref, nw_ref, ss_ref, ao_ref, x1_ref, h_ref):
        ao = _bdot(m_ref[...], w_ref[...])
        ao_ref[...] = ao
        x1 = x_ref[...] + g_ref[...] * ao
        x1_ref[...] = x1
        r = lax.rsqrt(jnp.mean(x1 * x1, axis=-1, keepdims=True) + EPS)
        s = ss_ref[0]
        h_ref[...] = ((x1 * r * nw_ref[...]) * (1.0 + s[1:2]) + s[0:1]).astype(BF16)

    row = pl.BlockSpec((tm, D), lambda i: (i, 0))
    f = jax.ShapeDtypeStruct((s_len, D), F32)
    return _pcall(body, name="out_proj_mod2", grid=(s_len // tm,),
                  in_specs=[row, _full((D, D)), row, _full((1, D)), _full((1, D)), _full((1, 2, D))],
                  out_specs=[row, row, row],
                  out_shape=[f, f, jax.ShapeDtypeStruct((s_len, D), BF16)])(mixed, w_o, x, g1, nw2, ss2)


TS = 1024


def _acc_call(body, *, name, grid, in_specs, out_specs, out_shape, acc_shapes, args, carry=None):
    return _pcall(body, name=name, grid=grid, in_specs=in_specs, out_specs=out_specs, out_shape=out_shape,
                  scratch=[pltpu.VMEM(s, F32) for s in acc_shapes], carry=carry)(*args)


def _mm_cs(a, w4, *, name):
    m, k = a.shape
    _, _, ns = w4.shape

    def body(a_ref, w_ref, o_ref):
        o_ref[...] = _bdot(a_ref[...], w_ref[0])

    return _pcall(body, name=name, grid=(m // TS, 4),
                  in_specs=[pl.BlockSpec((TS, k), lambda i, j: (i, 0)),
                            pl.BlockSpec((1, k, ns), lambda i, j: (j, 0, 0))],
                  out_specs=pl.BlockSpec((TS, ns), lambda i, j: (i, j)),
                  out_shape=jax.ShapeDtypeStruct((m, 4 * ns), F32))(a, w4)


def _mm_cs_nt(a, w4, *, name):
    m = a.shape[0]
    _, k, ns = w4.shape

    def body(a_ref, w_ref, o_ref, acc):
        j = pl.program_id(1)

        @pl.when(j == 0)
        def _():
            acc[...] = jnp.zeros_like(acc)

        acc[...] += _bdot(a_ref[...], w_ref[0], NT)

        @pl.when(j == 3)
        def _():
            o_ref[...] = acc[...]

    return _acc_call(body, name=name, grid=(m // TS, 4),
                     in_specs=[pl.BlockSpec((TS, ns), lambda i, j: (i, j)),
                               pl.BlockSpec((1, k, ns), lambda i, j: (j, 0, 0))],
                     out_specs=pl.BlockSpec((TS, k), lambda i, j: (i, 0)),
                     out_shape=jax.ShapeDtypeStruct((m, k), F32), acc_shapes=[(TS, k)], args=(a, w4))


def _mm_cs_tn(a, b, ns, *, name):
    s_len, k = a.shape
    nk = s_len // TS

    def body(a_ref, b_ref, o_ref, acc):
        t = pl.program_id(1)

        @pl.when(t == 0)
        def _():
            acc[...] = jnp.zeros_like(acc)

        acc[...] += _bdot(a_ref[...], b_ref[...], TN)

        @pl.when(t == nk - 1)
        def _():
            o_ref[0] = acc[...].astype(o_ref.dtype)

    return _acc_call(body, name=name, grid=(4, nk),
                     in_specs=[pl.BlockSpec((TS, k), lambda j, t: (t, 0)),
                               pl.BlockSpec((TS, ns), lambda j, t: (t, j))],
                     out_specs=pl.BlockSpec((1, k, ns), lambda j, t: (j, 0, 0)),
                     out_shape=jax.ShapeDtypeStruct((4, k, ns), BF16), acc_shapes=[(k, ns)], args=(a, b))


def _ffn_up(h2, g4, u4, carry=None):
    s_len = h2.shape[0]
    ns = g4.shape[1]

    def body(h_ref, g_ref, u_ref, a_ref, b_ref, z_ref):
        h = h_ref[...]
        a = _bdot(h, g_ref[0], NT)
        b = _bdot(h, u_ref[0], NT)
        a_ref[0] = a.astype(BF16)
        b_ref[0] = b.astype(BF16)
        z_ref[0] = (a * _sig(a) * b).astype(BF16)

    w = pl.BlockSpec((1, ns, D), lambda i, j: (j, 0, 0))
    o = pl.BlockSpec((1, TS, ns), lambda i, j: (j, i, 0))
    f = jax.ShapeDtypeStruct((4, s_len, ns), BF16)
    return _pcall(body, name="ffn_up", grid=(s_len // TS, 4),
                  in_specs=[pl.BlockSpec((TS, D), lambda i, j: (i, 0)), w, w], out_specs=[o, o, o],
                  out_shape=[f, f, jax.ShapeDtypeStruct((4, s_len, ns), BF16)], carry=carry)(h2, g4, u4)


def _ffn_down_loss(z4, dn4, x1, g2, tgt):
    _, s_len, ns = z4.shape

    def body(z_ref, w_ref, x1_ref, g_ref, t_ref, sq_ref, dx2_ref, dyb_ref, dg_ref, acc):
        i, j = pl.program_id(0), pl.program_id(1)

        @pl.when((i == 0) & (j == 0))
        def _():
            sq_ref[...] = jnp.zeros_like(sq_ref)
            dg_ref[...] = jnp.zeros_like(dg_ref)

        @pl.when(j == 0)
        def _():
            acc[...] = jnp.zeros_like(acc)

        acc[...] += _bdot(z_ref[0], w_ref[0])

        @pl.when(j == 3)
        def _():
            y_ = acc[...]
            g = g_ref[...]
            e = x1_ref[...] + g * y_ - t_ref[...]
            sq_ref[...] += jnp.sum(e * e, axis=0, keepdims=True)
            dx2 = e * (1.0 / D)
            dx2_ref[...] = dx2
            dyb_ref[...] = (g * dx2).astype(BF16)
            dg_ref[...] += jnp.sum(dx2 * y_, axis=0, keepdims=True)

    row = pl.BlockSpec((TS, D), lambda i, j: (i, 0))
    vec = _full((1, D))
    return _acc_call(body, name="ffn_down_loss", grid=(s_len // TS, 4),
                     in_specs=[pl.BlockSpec((1, TS, ns), lambda i, j: (j, i, 0)),
                               pl.BlockSpec((1, ns, D), lambda i, j: (j, 0, 0)), row, vec, row],
                     out_specs=[vec, row, row, vec],
                     out_shape=[jax.ShapeDtypeStruct((1, D), F32), jax.ShapeDtypeStruct((s_len, D), F32),
                                jax.ShapeDtypeStruct((s_len, D), BF16), jax.ShapeDtypeStruct((1, D), F32)],
                     acc_shapes=[(TS, D)], args=(z4, dn4, x1, g2, tgt))


def _ffn_dz(dyb, dn4, a4, b4):
    _, s_len, ns = a4.shape

    def body(dy_ref, w_ref, a_ref, b_ref, da_ref, db_ref):
        dz = _bdot(dy_ref[...], w_ref[0], NT)
        a = a_ref[0].astype(F32)
        s = _sig(a)
        da_ref[0] = (dz * b_ref[0].astype(F32) * (s * (1.0 + a * (1.0 - s)))).astype(BF16)
        db_ref[0] = (dz * (a * s)).astype(BF16)

    t = pl.BlockSpec((1, TS, ns), lambda i, j: (j, i, 0))
    o = jax.ShapeDtypeStruct((4, s_len, ns), BF16)
    return _pcall(body, name="ffn_dz", grid=(s_len // TS, 4),
                  in_specs=[pl.BlockSpec((TS, D), lambda i, j: (i, 0)),
                            pl.BlockSpec((1, ns, D), lambda i, j: (j, 0, 0)), t, t],
                  out_specs=[t, t], out_shape=[o, o])(dyb, dn4, a4, b4)


def _ffn_gdn(z4, dyb):
    _, s_len, ns = z4.shape
    nk = s_len // TS

    def body(z_ref, dy_ref, o_ref, acc):
        t = pl.program_id(1)

        @pl.when(t == 0)
        def _():
            acc[...] = jnp.zeros_like(acc)

        acc[...] += _bdot(z_ref[0], dy_ref[...], TN)

        @pl.when(t == nk - 1)
        def _():
            o_ref[0] = acc[...].astype(o_ref.dtype)

    return _acc_call(body, name="ffn_gdn", grid=(4, nk),
                     in_specs=[pl.BlockSpec((1, TS, ns), lambda j, t: (j, t, 0)),
                               pl.BlockSpec((TS, D), lambda j, t: (t, 0))],
                     out_specs=pl.BlockSpec((1, ns, D), lambda j, t: (j, 0, 0)),
                     out_shape=jax.ShapeDtypeStruct((4, ns, D), BF16), acc_shapes=[(ns, D)], args=(z4, dyb))


def _ffn_dh2(da4, db4, g4, u4, carry=None):
    _, s_len, ns = da4.shape

    def body(da_ref, db_ref, g_ref, u_ref, o_ref, acc):
        j = pl.program_id(1)

        @pl.when(j == 0)
        def _():
            acc[...] = jnp.zeros_like(acc)

        acc[...] += _bdot(da_ref[0], g_ref[0]) + _bdot(db_ref[0], u_ref[0])

        @pl.when(j == 3)
        def _():
            o_ref[...] = acc[...]

    t = pl.BlockSpec((1, TS, ns), lambda i, j: (j, i, 0))
    w = pl.BlockSpec((1, ns, D), lambda i, j: (j, 0, 0))
    return _acc_call(body, name="ffn_dh2", grid=(s_len // TS, 4), in_specs=[t, t, w, w],
                     out_specs=pl.BlockSpec((TS, D), lambda i, j: (i, 0)),
                     out_shape=jax.ShapeDtypeStruct((s_len, D), F32), acc_shapes=[(TS, D)],
                     args=(da4, db4, g4, u4), carry=carry)


def _ffn_ggu(h2, da4, db4, carry=None):
    _, s_len, ns = da4.shape
    nk = s_len // TS

    def body(h_ref, da_ref, db_ref, gg_ref, gu_ref, acc_g, acc_u):
        t = pl.program_id(1)

        @pl.when(t == 0)
        def _():
            acc_g[...] = jnp.zeros_like(acc_g)
            acc_u[...] = jnp.zeros_like(acc_u)

        h = h_ref[...]
        acc_g[...] += _bdot(da_ref[0], h, TN)
        acc_u[...] += _bdot(db_ref[0], h, TN)

        @pl.when(t == nk - 1)
        def _():
            gg_ref[0] = acc_g[...].astype(BF16)
            gu_ref[0] = acc_u[...].astype(BF16)

    d = pl.BlockSpec((1, TS, ns), lambda j, t: (j, t, 0))
    o = pl.BlockSpec((1, ns, D), lambda j, t: (j, 0, 0))
    f = jax.ShapeDtypeStruct((4, ns, D), BF16)
    return _acc_call(body, name="ffn_ggu", grid=(4, nk),
                     in_specs=[pl.BlockSpec((TS, D), lambda j, t: (t, 0)), d, d], out_specs=[o, o],
                     out_shape=[f, f], acc_shapes=[(ns, D), (ns, D)], args=(h2, da4, db4), carry=carry)


def _mod2_bwd(x1, dh2, dx2, ao, nw2, ss2, g1):
    s_len = x1.shape[0]

    def body(x1_ref, dh_ref, dx2_ref, ao_ref, nw_ref, ss_ref, g_ref,
             dx1_ref, da_ref, dss_ref, dnw_ref, dg_ref):
        i = pl.program_id(0)

        @pl.when(i == 0)
        def _():
            dss_ref[...] = jnp.zeros_like(dss_ref)
            dnw_ref[...] = jnp.zeros_like(dnw_ref)
            dg_ref[...] = jnp.zeros_like(dg_ref)

        dh = dh_ref[...]
        nw = nw_ref[...]
        scale = ss_ref[0][1:2]
        dxn, xh = _norm_bwd_rows(x1_ref[...], dh, nw, scale)
        dx1 = dx2_ref[...] + dxn
        dx1_ref[...] = dx1
        da_ref[...] = (g_ref[...] * dx1).astype(BF16)
        dg_ref[...] += jnp.sum(dx1 * ao_ref[...], axis=0, keepdims=True)
        dsh = jnp.sum(dh, axis=0, keepdims=True)
        dsc = jnp.sum(dh * xh * nw, axis=0, keepdims=True)
        dss_ref[...] += jnp.concatenate([dsh, dsc], axis=0)
        dnw_ref[...] += jnp.sum(dh * xh * (1.0 + scale), axis=0, keepdims=True)

    row = pl.BlockSpec((TM, D), lambda i: (i, 0))
    vec = _full((1, D))
    return _pcall(body, name="mod2_bwd", grid=(s_len // TM,),
                  in_specs=[row, row, row, row, vec, _full((1, 2, D)), vec],
                  out_specs=[row, row, _full((2, D)), vec, vec],
                  out_shape=[jax.ShapeDtypeStruct((s_len, D), F32), jax.ShapeDtypeStruct((s_len, D), BF16),
                             jax.ShapeDtypeStruct((2, D), F32), jax.ShapeDtypeStruct((1, D), F32),
                             jax.ShapeDtypeStruct((1, D), F32)])(x1, dh2, dx2, ao, nw2, ss2, g1)


def _mod1_bwd(ctx, x, dh, dx1, nw1, ss1, carry=None):
    s_len = dx1.shape[0]
    tt = L + s_len

    def body(c_ref, x_ref, dh_ref, dx1_ref, nw_ref, ss_ref, dx_ref, dss_ref, dnw_ref):
        i = pl.program_id(0)
        tok = jnp.where(i == 0, c_ref[...], x_ref[...])

        @pl.when(i == 0)
        def _():
            dnw_ref[...] = jnp.zeros_like(dnw_ref)

        @pl.when(i <= 1)
        def _():
            dss_ref[...] = jnp.zeros_like(dss_ref)

        dh_ = dh_ref[...]
        nw = nw_ref[...]
        scale = ss_ref[0][1:2]
        dxn, xh = _norm_bwd_rows(tok, dh_, nw, scale)

        @pl.when(i >= 1)
        def _():
            dx_ref[...] = dx1_ref[...] + dxn

        dsh = jnp.sum(dh_, axis=0, keepdims=True)
        dsc = jnp.sum(dh_ * xh * nw, axis=0, keepdims=True)
        dss_ref[...] += jnp.concatenate([dsh, dsc], axis=0)[None]
        dnw_ref[...] += jnp.sum(dh_ * xh * (1.0 + scale), axis=0, keepdims=True)

    row = pl.BlockSpec((TM, D), lambda i: (i, 0))
    lat = pl.BlockSpec((TM, D), lambda i: (jnp.maximum(i - 1, 0), 0))
    sel = pl.BlockSpec((1, 2, D), lambda i: (jnp.minimum(i, 1), 0, 0))
    return _pcall(body, name="mod1_bwd", grid=(tt // TM,),
                  in_specs=_tok_specs() + [row, lat, _full((1, D)), sel],
                  out_specs=[lat, sel, _full((1, D))],
                  out_shape=[jax.ShapeDtypeStruct((s_len, D), F32), jax.ShapeDtypeStruct((2, 2, D), F32),
                             jax.ShapeDtypeStruct((1, D), F32)], carry=carry)(ctx, x, dh, dx1, nw1, ss1)


def _rows(c):
    return slice(c * CH, (c + 1) * CH)


def _chunk_masks(rev, transpose=False):
    r = lax.broadcasted_iota(jnp.int32, (TM, TM), 0)
    c = lax.broadcasted_iota(jnp.int32, (TM, TM), 1)
    same = (r // CH) == (c // CH)
    before = (c >= r) if (rev != transpose) else (c <= r)
    return same & before, same


def _chunk_scan(x, rev, transpose=False):
    r = lax.broadcasted_iota(jnp.int32, (CH, CH), 0)
    c = lax.broadcasted_iota(jnp.int32, (CH, CH), 1)
    tri = ((c >= r) if (rev != transpose) else (c <= r)).astype(F32)
    return jnp.concatenate([_dot(tri, x[_rows(ch)], prec=HI) for ch in range(x.shape[0] // CH)], axis=0)


def _chunk_total(x):
    return jnp.concatenate([jnp.broadcast_to(jnp.sum(x[_rows(ch)], axis=0, keepdims=True), (CH, x.shape[1]))
                            for ch in range(x.shape[0] // CH)], axis=0)


def _hgrn_gate(fl, qraw, lg):
    lb = 1.0 / (1.0 + jnp.exp(lg[1:2] - lg[0:1]))
    sg = _sig(fl)
    f = lb + (1.0 - lb) * sg
    q = qraw * _sig(qraw) * (HGD ** -0.5)
    return lb, sg, f, q


def _hgrn_fwd(p, lg, *, rev, carry=None):
    tt = p.shape[0]
    nt = tt // TM
    ncht = TM // CH
    d = 1 if rev else 0

    def tile_of(s):
        return jnp.where(s == 0, 0, nt - s) if rev else s

    def body(f_ref, inp_ref, q_ref, lg_ref, o_ref, st_ref, state):
        s = pl.program_id(0)

        @pl.when(s == 0)
        def _():
            state[...] = jnp.zeros_like(state)

        _, _, f, q = _hgrn_gate(f_ref[...], q_ref[...], lg_ref[0])
        lf = jnp.log(f)
        causal, _ = _chunk_masks(rev)
        cum = _chunk_scan(lf, rev)
        tot = _chunk_total(lf)
        qd = (q * jnp.exp(cum)).astype(BF16)
        kd = ((1.0 - f) * jnp.exp(-cum)).astype(BF16)
        ke = ((1.0 - f) * jnp.exp(tot - cum)).astype(BF16)
        et = jnp.exp(tot)
        v = inp_ref[...].astype(BF16)
        order = range(ncht - 1, -1, -1) if rev else range(ncht)
        outs = []
        for h in range(4):
            sl = slice(h * HGD, (h + 1) * HGD)
            qd_, kd_, ke_, v_ = qd[:, sl], kd[:, sl], ke[:, sl], v[:, sl]
            pm = jnp.where(causal, _dot(qd_, kd_, NT), 0.0).astype(BF16)
            o_h = _dot(pm, v_)
            upd = [_dot(v_[_rows(c)], ke_[_rows(c)], TN) for c in range(ncht)]
            st = state[h]
            for c in order:
                st_ref[c, h] = st
                st = st * et[c * CH:c * CH + 1, sl] + upd[c]
            state[h] = st
            inter = [_dot(qd_[_rows(c)], st_ref[c, h].astype(BF16), NT) for c in range(ncht)]
            outs.append(o_h + jnp.concatenate(inter, axis=0))
        o_ref[...] = jnp.concatenate(outs, axis=1)

    def col(cb):
        return pl.BlockSpec((TM, HGW), lambda s: (tile_of(s), cb))

    return _pcall(
        body, name="hgrn_fwd_rev" if rev else "hgrn_fwd", grid=(nt,),
        in_specs=[col(C_FB if rev else C_FF), col(C_INP), col(C_QHG),
                  pl.BlockSpec((1, 2, HGW), lambda s: (d, 0, 0))],
        out_specs=[pl.BlockSpec((TM, HGW), lambda s: (tile_of(s), 0)),
                   pl.BlockSpec((ncht, 4, HGD, HGD), lambda s: (tile_of(s), 0, 0, 0))],
        out_shape=[jax.ShapeDtypeStruct((tt, HGW), F32),
                   jax.ShapeDtypeStruct((nt * ncht, 4, HGD, HGD), F32)],
        scratch=[pltpu.VMEM((4, HGD, HGD), F32)], carry=carry)(p, p, p, lg)


def _hgrn_bwd(p, lg, do, st, dp, prev, *, rev, carry=None):
    tt = p.shape[0]
    nt = tt // TM
    ncht = TM // CH
    d = 1 if rev else 0
    second = prev is not None

    def tile_of(s):
        return jnp.where(s == nt - 1, 0, s + 1) if rev else nt - 1 - s

    def body(*refs):
        if second:
            (f_ref, inp_ref, q_ref, lg_ref, do_ref, st_ref, dvp_ref, dqp_ref, _dp_in,
             dp_ref, dlg_ref, dstate) = refs
        else:
            (f_ref, inp_ref, q_ref, lg_ref, do_ref, st_ref, _dp_in,
             dp_ref, dv_ref, dq_ref, dlg_ref, dstate) = refs
        s = pl.program_id(0)
        tile = tile_of(s)

        @pl.when(s == 0)
        def _():
            dstate[...] = jnp.zeros_like(dstate)
            dlg_ref[...] = jnp.zeros_like(dlg_ref)

        qraw = q_ref[...]
        lb, sg, f, q = _hgrn_gate(f_ref[...], qraw, lg_ref[0])
        lf = jnp.log(f)
        causal, _ = _chunk_masks(rev)
        causal_t, _ = _chunk_masks(rev, transpose=True)
        cum = _chunk_scan(lf, rev)
        tot = _chunk_total(lf)
        ea, eb, ee, et = jnp.exp(cum), jnp.exp(-cum), jnp.exp(tot - cum), jnp.exp(tot)
        qdf, kdf, kef = q * ea, (1.0 - f) * eb, (1.0 - f) * ee
        qd, kd, ke = qdf.astype(BF16), kdf.astype(BF16), kef.astype(BF16)
        v = inp_ref[...].astype(BF16)
        dob = jnp.where(tile == 0, 0.0, do_ref[...]).astype(BF16)
        order = range(ncht) if rev else range(ncht - 1, -1, -1)
        dq_l, dk_l, dv_l, dcum_l, dtot_l = [], [], [], [], []
        for h in range(4):
            sl = slice(h * HGD, (h + 1) * HGD)
            qd_, kd_, ke_, v_, do_ = qd[:, sl], kd[:, sl], ke[:, sl], v[:, sl], dob[:, sl]
            pmt = jnp.where(causal_t, _dot(kd_, qd_, NT), 0.0).astype(BF16)
            dpm = jnp.where(causal, _dot(do_, v_, NT), 0.0).astype(BF16)
            dpmt = jnp.where(causal_t, _dot(v_, do_, NT), 0.0).astype(BF16)
            dv = _dot(pmt, do_)
            dqd = _dot(dpm, kd_)
            dkd = _dot(dpmt, qd_)
            upd = [_dot(do_[_rows(c)], qd_[_rows(c)], TN) for c in range(ncht)]
            ds = dstate[h]
            ds1 = [None] * ncht
            for c in order:
                ds1[c] = ds
                ds = ds * et[c * CH:c * CH + 1, sl] + upd[c]
            dstate[h] = ds
            dke_c, dv_c, dqd_c, dtot_c = [], [], [], []
            for c in range(ncht):
                st0 = st_ref[c, h]
                dsb = ds1[c].astype(BF16)
                dke_ = _dot(v_[_rows(c)], dsb)
                dke_c.append(dke_)
                dv_c.append(_dot(ke_[_rows(c)], dsb, NT))
                dqd_c.append(_dot(do_[_rows(c)], st0.astype(BF16)))
                dt = (jnp.sum(ds1[c] * st0, axis=0, keepdims=True) * et[c * CH:c * CH + 1, sl]
                      + jnp.sum(dke_ * kef[_rows(c), sl], axis=0, keepdims=True))
                dtot_c.append(jnp.broadcast_to(dt, (CH, HGD)))
            dke = jnp.concatenate(dke_c, axis=0)
            dqd = dqd + jnp.concatenate(dqd_c, axis=0)
            dv_l.append(dv + jnp.concatenate(dv_c, axis=0))
            dtot_l.append(jnp.concatenate(dtot_c, axis=0))
            dq_l.append(dqd * ea[:, sl])
            dk_l.append(dkd * eb[:, sl] + dke * ee[:, sl])
            dcum_l.append(dqd * qdf[:, sl] - dkd * kdf[:, sl] - dke * kef[:, sl])
        dcum = jnp.concatenate(dcum_l, axis=1)
        dlf = _chunk_scan(dcum, rev, transpose=True) + jnp.concatenate(dtot_l, axis=1)
        dq_t = jnp.concatenate(dq_l, axis=1)
        dv_t = jnp.concatenate(dv_l, axis=1)

        df = dlf / f - jnp.concatenate(dk_l, axis=1)
        dfl = df * (1.0 - lb) * sg * (1.0 - sg)
        dlb = jnp.sum(df * (1.0 - sg), axis=0, keepdims=True)
        dl0 = dlb * lb * (1.0 - lb)
        dlg_ref[...] += jnp.concatenate([dl0, -dl0], axis=0)[None]
        if second:
            sq = _sig(qraw)
            dqr = (dqp_ref[...] + dq_t) * (HGD ** -0.5) * (sq * (1.0 + qraw * (1.0 - sq)))
            dp_ref[...] = jnp.concatenate([dfl, dvp_ref[...] + dv_t, dqr], axis=1).astype(BF16)
        else:
            dp_ref[...] = dfl.astype(BF16)
            dv_ref[...] = dv_t
            dq_ref[...] = dq_t

    def col(cb):
        return pl.BlockSpec((TM, HGW), lambda s: (tile_of(s), cb))

    tok = pl.BlockSpec((TM, HGW), lambda s: (tile_of(s), 0))
    in_specs = [col(C_FB if rev else C_FF), col(C_INP), col(C_QHG),
                pl.BlockSpec((1, 2, HGW), lambda s: (d, 0, 0)),
                pl.BlockSpec((TM, HGW), lambda s: (jnp.maximum(tile_of(s) - 1, 0), 0)),
                pl.BlockSpec((ncht, 4, HGD, HGD), lambda s: (tile_of(s), 0, 0, 0))]
    args = [p, p, p, lg, do, st]
    dlg_spec = _full((1, 2, HGW))
    dlg_shape = jax.ShapeDtypeStruct((1, 2, HGW), F32)
    if second:
        in_specs += [tok, tok]
        args += [prev[0], prev[1]]
        out_specs = [pl.BlockSpec((TM, 3 * HGW), lambda s: (tile_of(s), 0)), dlg_spec]
        out_shape = [jax.ShapeDtypeStruct(dp.shape, BF16), dlg_shape]
    else:
        out_specs = [pl.BlockSpec((TM, HGW), lambda s: (tile_of(s), C_FB if rev else C_FF)), tok, tok, dlg_spec]
        out_shape = [jax.ShapeDtypeStruct(dp.shape, BF16), jax.ShapeDtypeStruct((tt, HGW), F32),
                     jax.ShapeDtypeStruct((tt, HGW), F32), dlg_shape]
    in_specs.append(ANY)
    args.append(dp)
    return _pcall(body, name="hgrn_bwd_rev" if rev else "hgrn_bwd", grid=(nt,),
                  in_specs=in_specs, out_specs=out_specs, out_shape=out_shape,
                  scratch=[pltpu.VMEM((4, HGD, HGD), F32)],
                  aliases={len(args) - 1: 0}, carry=carry)(*args)


def _head_rms(o, w, nheads):
    outs = []
    for h in range(nheads):
        oh = o[:, h * HGD:(h + 1) * HGD]
        outs.append(oh * lax.rsqrt(jnp.mean(oh * oh, axis=-1, keepdims=True) + EPS))
    return jnp.concatenate(outs, axis=1)


def _readout(o0, o1, p, hw4):
    s_len = o0.shape[0] - L

    def body(o0_ref, o1_ref, g_ref, w_ref, y_ref):
        xh = _head_rms(o0_ref[...] + o1_ref[...], None, 4)
        g = g_ref[...]
        y_ref[...] = (xh * w_ref[...] * (g * _sig(g))).astype(BF16)

    lat = pl.BlockSpec((TM, HGW), lambda i: (i + 1, 0))
    return _pcall(body, name="readout", grid=(s_len // TM,),
                  in_specs=[lat, lat, pl.BlockSpec((TM, HGW), lambda i: (i + 1, C_GHG)), _full((1, HGW))],
                  out_specs=pl.BlockSpec((TM, HGW), lambda i: (i, 0)),
                  out_shape=jax.ShapeDtypeStruct((s_len, HGW), BF16))(o0, o1, p, hw4)


def _readout_bwd(o0, o1, p, hw4, dy, dp, carry=None):
    tt = o0.shape[0]
    s_len = tt - L

    def body(o0_ref, o1_ref, g_ref, w_ref, dy_ref, _dp_in, dp_ref, do_ref, dw_ref):
        i = pl.program_id(0)

        @pl.when(i == 0)
        def _():
            dw_ref[...] = jnp.zeros_like(dw_ref)
            dp_ref[...] = jnp.zeros_like(dp_ref)

        @pl.when(i >= 1)
        def _():
            o = o0_ref[...] + o1_ref[...]
            g = g_ref[...]
            w = w_ref[...]
            sg = _sig(g)
            dy_ = dy_ref[...]
            dsw = dy_ * (g * sg)
            outs, xhs = [], []
            for h in range(4):
                sl = slice(h * HGD, (h + 1) * HGD)
                oh = o[:, sl]
                r = lax.rsqrt(jnp.mean(oh * oh, axis=-1, keepdims=True) + EPS)
                xh = oh * r
                dxh = dsw[:, sl] * w[:, sl]
                outs.append(r * (dxh - xh * jnp.mean(dxh * xh, axis=-1, keepdims=True)))
                xhs.append(xh)
            xh = jnp.concatenate(xhs, axis=1)
            do_ref[...] = jnp.concatenate(outs, axis=1)
            dp_ref[...] = (dy_ * xh * w * (sg * (1.0 + g * (1.0 - sg)))).astype(BF16)
            dw_ref[...] += jnp.sum(dsw * xh, axis=0, keepdims=True)

    tok = pl.BlockSpec((TM, HGW), lambda i: (i, 0))
    lat = pl.BlockSpec((TM, HGW), lambda i: (jnp.maximum(i - 1, 0), 0))
    return _pcall(body, name="readout_bwd", grid=(tt // TM,),
                  in_specs=[tok, tok, pl.BlockSpec((TM, HGW), lambda i: (i, C_GHG)), _full((1, HGW)), lat, ANY],
                  out_specs=[pl.BlockSpec((TM, HGW), lambda i: (i, C_GHG)), lat, _full((1, HGW))],
                  out_shape=[jax.ShapeDtypeStruct(dp.shape, BF16), jax.ShapeDtypeStruct((s_len, HGW), F32),
                             jax.ShapeDtypeStruct((1, HGW), F32)],
                  aliases={5: 0}, carry=carry)(o0, o1, p, hw4, dy, dp)


def _rope_tables(s_len):
    t = np.arange(s_len)
    inv = ROPE_THETA ** (-np.arange(0, 32, 2, dtype=np.float64) / 32)
    def half(pos):
        ang = pos[:, None].astype(np.float64) * inv[None, :]
        return (np.concatenate([np.cos(ang), np.cos(ang)], 1), np.concatenate([-np.sin(ang), np.sin(ang)], 1))
    cr, sr = half(t // GRID_W)
    cc, sc = half(t % GRID_W)
    cos = np.concatenate([cr, cc, cr, cc], 1)
    sin = np.concatenate([sr, sc, sr, sc], 1)
    cos = np.concatenate([np.ones((L, 128)), cos], 0)
    sin = np.concatenate([np.zeros((L, 128)), sin], 0)
    return jnp.asarray(cos, F32), jnp.asarray(sin, F32)


def _blockdiag(n, w):
    i = np.arange(n)
    return jnp.asarray((i[:, None] // w == i[None, :] // w) / float(w), F32)


def _dup_matrix():
    m = np.zeros((128, 512), np.float32)
    for g in range(2):
        for j in range(4):
            for dd in range(HDIM):
                m[64 * g + dd, 256 * g + 64 * j + dd] = 1.0
    return m


def _head_mean(x, blockdiag):
    return _dot(x, blockdiag, prec=lax.Precision.HIGH)


def _rot(x):
    n = x.shape[1]
    lane = lax.broadcasted_iota(jnp.int32, x.shape, 1)
    return jnp.where((lane % 32) < 16, pltpu.roll(x, n - 16, 1), pltpu.roll(x, 16, 1))


def _qk_prep(p, cos, sin, qnw8, knw2, bd512, bd128, dup):
    tt = p.shape[0]

    def body(q_ref, kv_ref, cos_ref, sin_ref, qw_ref, kw_ref, b5_ref, b1_ref, dup_ref,
             qr_ref, k4_ref, v4_ref):
        cos_, sin_ = cos_ref[...], sin_ref[...]
        q = q_ref[...]
        qn = q * lax.rsqrt(_head_mean(q * q, b5_ref[...]) + EPS) * qw_ref[...]
        cos4 = jnp.concatenate([cos_] * 4, axis=1)
        sin4 = jnp.concatenate([sin_] * 4, axis=1)
        qr_ref[...] = ((qn * cos4 + _rot(qn) * sin4) * (HDIM ** -0.5)).astype(BF16)
        kv = kv_ref[...]
        k, v = kv[:, :128], kv[:, 128:]
        kn = k * lax.rsqrt(_head_mean(k * k, b1_ref[...]) + EPS) * kw_ref[...]
        kr = kn * cos_ + _rot(kn) * sin_
        k4_ref[...] = _bdot(kr, dup_ref[...]).astype(BF16)
        v4_ref[...] = _bdot(v, dup_ref[...]).astype(BF16)

    row = lambda w, cb: pl.BlockSpec((TM, w), lambda i: (i, cb))
    out = jax.ShapeDtypeStruct((tt, ATW), BF16)
    return _pcall(body, name="qk_prep", grid=(tt // TM,),
                  in_specs=[row(ATW, C_QRAW), row(256, C_KV), row(128, 0), row(128, 0),
                            _full((1, ATW)), _full((1, 128)), _full((ATW, ATW)), _full((128, 128)),
                            _full((128, ATW))],
                  out_specs=[row(ATW, 0)] * 3, out_shape=[out] * 3)(
                      p, p, cos, sin, qnw8, knw2, bd512, bd128, dup)


def _attn_masks(i, nb):
    r = lax.broadcasted_iota(jnp.int32, (4 * BLK, 3 * BLK + L), 0) % BLK
    c = lax.broadcasted_iota(jnp.int32, (4 * BLK, 3 * BLK + L), 1)
    kpos = (i - 1) * BLK + c
    loc = (jnp.abs(c - BLK - r) <= BLK) & (kpos >= 0) & (kpos < nb * BLK)
    return loc | (c >= 3 * BLK)


def _stack_mask():
    r = lax.broadcasted_iota(jnp.int32, (4 * BLK, 256), 0)
    lane = lax.broadcasted_iota(jnp.int32, (4 * BLK, 256), 1)
    return (r // BLK) == (lane // HDIM)


def _stack_heads(xg, fill=0.0):
    x4 = jnp.concatenate([xg] * 4, axis=0)
    return jnp.where(_stack_mask(), x4, jnp.full_like(x4, fill))


def _unstack_heads(x4):
    out = jnp.where(_lane_mask(0), x4[0:BLK], 0.0)
    for j in range(1, 4):
        out = out + jnp.where(_lane_mask(j), x4[j * BLK:(j + 1) * BLK], 0.0)
    return out


def _per_head_rows(vals):
    return jnp.concatenate([jnp.broadcast_to(v, (BLK, 1)) for v in vals], axis=0)


def _lane_mask(j):
    lane = lax.broadcasted_iota(jnp.int32, (1, 256), 1)
    return (lane // HDIM) == j


def _attn_specs(nb):
    blk = lambda off: pl.BlockSpec((BLK, ATW), lambda i: (jnp.clip(i + off, 0, nb - 1) + 2, 0))
    ctx = pl.BlockSpec((L, ATW), lambda i: (0, 0))
    return blk, ctx


def _attn_fwd(qr, k4, v4, sinks, carry=None):
    tt = qr.shape[0]
    s_len = tt - L
    nb = s_len // BLK

    def body(sk_ref, q_ref, kp, ko, kn, kc, vp, vo, vn, vc, y_ref, lse_ref):
        i = pl.program_id(0)
        valid = _attn_masks(i, nb)
        q = q_ref[...]
        ys, lses = [], []
        for g in range(2):
            gs = slice(256 * g, 256 * g + 256)
            kcat = jnp.concatenate([kp[:, gs], ko[:, gs], kn[:, gs], kc[:, gs]], axis=0)
            vcat = jnp.concatenate([vp[:, gs], vo[:, gs], vn[:, gs], vc[:, gs]], axis=0)
            sink = _per_head_rows([sk_ref[4 * g + j] for j in range(4)])
            s = jnp.where(valid, _dot(_stack_heads(q[:, gs]), kcat, NT), -1e30)
            m = jnp.maximum(jnp.max(s, axis=-1, keepdims=True), sink)
            e = jnp.exp(s - m)
            den = jnp.sum(e, axis=-1, keepdims=True) + jnp.exp(sink - m)
            ys.append(_unstack_heads(_bdot(e * (1.0 / den), vcat)))
            lses.append(_unstack_heads(jnp.broadcast_to(m + jnp.log(den), (4 * BLK, 256))))
        y_ref[...] = jnp.concatenate(ys, axis=1).astype(BF16)
        lse_ref[...] = jnp.concatenate(lses, axis=1)

    blk, ctx = _attn_specs(nb)
    out = pl.BlockSpec((BLK, ATW), lambda i: (i, 0))
    return _pcall(body, name="attn_fwd", grid=(nb,),
                  in_specs=[pl.BlockSpec(memory_space=pltpu.SMEM), blk(0),
                            blk(-1), blk(0), blk(1), ctx, blk(-1), blk(0), blk(1), ctx],
                  out_specs=[out, out],
                  out_shape=[jax.ShapeDtypeStruct((s_len, ATW), BF16),
                             jax.ShapeDtypeStruct((s_len, ATW), F32)], carry=carry)(
                      sinks, qr, k4, k4, k4, k4, v4, v4, v4, v4)


def _attn_bwd(qr, k4, v4, sinks, y, lse, dy, carry=None):
    tt = qr.shape[0]
    s_len = tt - L
    nb = s_len // BLK

    def body(sk_ref, q_ref, kp, ko, kn, kc, vp, vo, vn, vc, y_ref, lse_ref, dy_ref,
             dq_ref, dkw_ref, dvw_ref, dkc_ref, dvc_ref, dsk_ref):
        i = pl.program_id(0)

        @pl.when(i == 0)
        def _():
            dkc_ref[...] = jnp.zeros_like(dkc_ref)
            dvc_ref[...] = jnp.zeros_like(dvc_ref)
            dsk_ref[...] = jnp.zeros_like(dsk_ref)

        valid = _attn_masks(i, nb)
        q = q_ref[...]
        dy_ = dy_ref[...]
        dly = dy_ * y_ref[...].astype(F32)
        lse_ = lse_ref[...]
        dqs = []
        for g in range(2):
            gs = slice(256 * g, 256 * g + 256)
            kcat = jnp.concatenate([kp[:, gs], ko[:, gs], kn[:, gs], kc[:, gs]], axis=0)
            vcat = jnp.concatenate([vp[:, gs], vo[:, gs], vn[:, gs], vc[:, gs]], axis=0)
            q4 = _stack_heads(q[:, gs])
            dy4 = _stack_heads(dy_[:, gs]).astype(BF16)
            lse4 = jnp.max(_stack_heads(lse_[:, gs], fill=-1e30), axis=-1, keepdims=True)
            delta = jnp.sum(_stack_heads(dly[:, gs]), axis=-1, keepdims=True)
            sink = _per_head_rows([sk_ref[4 * g + j] for j in range(4)])
            pr = jnp.where(valid, jnp.exp(_dot(q4, kcat, NT) - lse4), 0.0)
            dsb = (pr * (_dot(dy4, vcat, NT) - delta)).astype(BF16)
            dsink = jnp.exp(sink - lse4) * delta
            for j in range(4):
                dsk_ref[4 * g + j:4 * g + j + 1, :] += jnp.broadcast_to(
                    -jnp.sum(dsink[j * BLK:(j + 1) * BLK], axis=0, keepdims=True), (1, 128))
            dqs.append(_unstack_heads(_dot(dsb, kcat)))
            dkg = _dot(dsb, q4, TN)
            dvg = _dot(pr.astype(BF16), dy4, TN)
            dkw_ref[0, :, gs] = dkg[:3 * BLK]
            dvw_ref[0, :, gs] = dvg[:3 * BLK]
            dkc_ref[:, gs] += dkg[3 * BLK:]
            dvc_ref[:, gs] += dvg[3 * BLK:]
        dq_ref[...] = jnp.concatenate(dqs, axis=1)

    blk, ctx = _attn_specs(nb)
    out = pl.BlockSpec((BLK, ATW), lambda i: (i, 0))
    win = pl.BlockSpec((1, 3 * BLK, ATW), lambda i: (i, 0, 0))
    acc = _full((L, ATW))
    return _pcall(body, name="attn_bwd", grid=(nb,),
                  in_specs=[pl.BlockSpec(memory_space=pltpu.SMEM), blk(0),
                            blk(-1), blk(0), blk(1), ctx, blk(-1), blk(0), blk(1), ctx, out, out, out],
                  out_specs=[out, win, win, acc, acc, _full((8, 128))],
                  out_shape=[jax.ShapeDtypeStruct((s_len, ATW), F32),
                             jax.ShapeDtypeStruct((nb, 3 * BLK, ATW), F32),
                             jax.ShapeDtypeStruct((nb, 3 * BLK, ATW), F32),
                             jax.ShapeDtypeStruct((L, ATW), F32), jax.ShapeDtypeStruct((L, ATW), F32),
                             jax.ShapeDtypeStruct((8, 128), F32)], carry=carry)(
                      sinks, qr, k4, k4, k4, k4, v4, v4, v4, v4, y, lse, dy)


def _attn_post(p, cos, sin, qnw8, knw2, bd512, bd128, dupt, dq, dkw, dvw, dkc, dvc, dp, carry=None):
    tt = p.shape[0]
    s_len = tt - L
    nb = s_len // BLK
    nctx = L // BLK

    def body(q_ref, kv_ref, cos_ref, sin_ref, qw_ref, kw_ref, b5_ref, b1_ref, dupt_ref,
             dq_ref, kwp, kwo, kwn, vwp, vwo, vwn, dkc_ref, dvc_ref, _dp_in,
             dp_ref, dqw_ref, dkw_ref):
        t = pl.program_id(0)
        j = t - nctx

        @pl.when(t == 0)
        def _():
            dqw_ref[...] = jnp.zeros_like(dqw_ref)
            dkw_ref[...] = jnp.zeros_like(dkw_ref)

        is_lat = t >= nctx
        cos_, sin_ = cos_ref[...], sin_ref[...]
        has_p = is_lat & (j >= 1)
        has_n = is_lat & (j <= nb - 2)
        dk4 = (jnp.where(is_lat, kwo[0], dkc_ref[...]) + jnp.where(has_p, kwp[0], 0.0)
               + jnp.where(has_n, kwn[0], 0.0))
        dv4 = (jnp.where(is_lat, vwo[0], dvc_ref[...]) + jnp.where(has_p, vwp[0], 0.0)
               + jnp.where(has_n, vwn[0], 0.0))
        dkr = _dot(dk4, dupt_ref[...], prec=HI)
        dv = _dot(dv4, dupt_ref[...], prec=HI)
        kv = kv_ref[...]
        k = kv[:, :128]
        kw = kw_ref[...]
        rk = lax.rsqrt(_head_mean(k * k, b1_ref[...]) + EPS)
        xk = k * rk
        dkn = dkr * cos_ + _rot(dkr * sin_)
        dxk = dkn * kw
        dk = rk * (dxk - xk * _head_mean(dxk * xk, b1_ref[...]))
        dkw_ref[...] += jnp.sum(dkn * xk, axis=0, keepdims=True)
        q = q_ref[...]
        qw = qw_ref[...]
        rq = lax.rsqrt(_head_mean(q * q, b5_ref[...]) + EPS)
        xq = q * rq
        cos4 = jnp.concatenate([cos_] * 4, axis=1)
        sin4 = jnp.concatenate([sin_] * 4, axis=1)
        dqr = jnp.where(is_lat, dq_ref[...], 0.0) * (HDIM ** -0.5)
        dqn = dqr * cos4 + _rot(dqr * sin4)
        dxq = dqn * qw
        dqraw = rq * (dxq - xq * _head_mean(dxq * xq, b5_ref[...]))
        dqw_ref[...] += jnp.sum(dqn * xq, axis=0, keepdims=True)
        dp_ref[...] = jnp.concatenate([dqraw, dk, dv], axis=1).astype(BF16)

    row = lambda w, cb: pl.BlockSpec((BLK, w), lambda t: (t, cb))
    lat = pl.BlockSpec((BLK, ATW), lambda t: (jnp.maximum(t - nctx, 0), 0))

    def part(off):
        return pl.BlockSpec((1, BLK, ATW), lambda t: (jnp.clip(t - nctx + off, 0, nb - 1), 1 - off, 0))

    cacc = pl.BlockSpec((BLK, ATW), lambda t: (jnp.minimum(t, nctx - 1), 0))
    return _pcall(body, name="attn_post", grid=(tt // BLK,),
                  in_specs=[row(ATW, C_QRAW), row(256, C_KV), row(128, 0), row(128, 0),
                            _full((1, ATW)), _full((1, 128)), _full((ATW, ATW)), _full((128, 128)),
                            _full((ATW, 128)), lat, part(-1), part(0), part(1), part(-1), part(0), part(1),
                            cacc, cacc, ANY],
                  out_specs=[pl.BlockSpec((BLK, 768), lambda t: (t, C_QKV)), _full((1, ATW)), _full((1, 128))],
                  out_shape=[jax.ShapeDtypeStruct(dp.shape, BF16), jax.ShapeDtypeStruct((1, ATW), F32),
                             jax.ShapeDtypeStruct((1, 128), F32)],
                  aliases={18: 0}, carry=carry)(p, p, cos, sin, qnw8, knw2, bd512, bd128, dupt,
                                   dq, dkw, dkw, dkw, dvw, dvw, dvw, dkc, dvc, dp)


def _merge(ah, aa, p):
    s_len = ah.shape[0]

    def body(ah_ref, aa_ref, gh_ref, ga_ref, m_ref):
        m_ref[...] = (_sig(gh_ref[...]) * ah_ref[...] + _sig(ga_ref[...]) * aa_ref[...]).astype(BF16)

    row = pl.BlockSpec((TM, D), lambda i: (i, 0))
    return _pcall(body, name="merge", grid=(s_len // TM,),
                  in_specs=[row, row, pl.BlockSpec((TM, D), lambda i: (i + 1, 2)),
                            pl.BlockSpec((TM, D), lambda i: (i + 1, 3))],
                  out_specs=row, out_shape=jax.ShapeDtypeStruct((s_len, D), BF16))(ah, aa, p, p)


def _merge_bwd(dm, ah, aa, p, carry=None):
    tt = p.shape[0]
    s_len = tt - L

    def body(dm_ref, ah_ref, aa_ref, gh_ref, ga_ref, dp_ref, dmh_ref, dma_ref):
        i = pl.program_id(0)

        @pl.when(i == 0)
        def _():
            dp_ref[...] = jnp.zeros_like(dp_ref)

        @pl.when(i >= 1)
        def _():
            dm_ = dm_ref[...]
            sh, sa = _sig(gh_ref[...]), _sig(ga_ref[...])
            dp_ref[...] = jnp.concatenate([dm_ * ah_ref[...] * sh * (1.0 - sh),
                                           dm_ * aa_ref[...] * sa * (1.0 - sa)], axis=1).astype(BF16)
            dmh_ref[...] = (dm_ * sh).astype(BF16)
            dma_ref[...] = (dm_ * sa).astype(BF16)

    lat = pl.BlockSpec((TM, D), lambda i: (jnp.maximum(i - 1, 0), 0))
    return _pcall(body, name="merge_bwd", grid=(tt // TM,),
                  in_specs=[lat, lat, lat, pl.BlockSpec((TM, D), lambda i: (i, 2)),
                            pl.BlockSpec((TM, D), lambda i: (i, 3))],
                  out_specs=[pl.BlockSpec((TM, 2 * D), lambda i: (i, C_GATES)), lat, lat],
                  out_shape=[jax.ShapeDtypeStruct((tt, NCOL), BF16), jax.ShapeDtypeStruct((s_len, D), BF16),
                             jax.ShapeDtypeStruct((s_len, D), BF16)], carry=carry)(dm, ah, aa, p, p)


def _local_step(x, ctx, tgt, mod, modc, nw1, nw2, lg, hw, qnw, knw, sinks,
                w_in, wts, dist=None):
    s_len = x.shape[0]
    tt = s_len + L
    ss1 = jnp.stack([modc, mod[0:2]])
    ss2 = mod[3:5][None]
    g1, g2 = mod[2:3], mod[5:6]
    hw4 = jnp.tile(hw, (1, 4))
    qnw8 = jnp.tile(qnw, (1, 8))
    knw2 = jnp.tile(knw, (1, 2))
    cos, sin = _rope_tables(s_len)
    bd512, bd128 = _blockdiag(ATW, HDIM), _blockdiag(128, HDIM)
    dupm = _dup_matrix()
    dup, dupt = jnp.asarray(dupm, BF16), jnp.asarray(dupm.T, F32)
    tmt = tt

    def four(b):
        return b.reshape(4, 2 * b.shape[1], b.shape[2])

    def halves(g):
        return g.reshape(4, 2, g.shape[1] // 2, g.shape[2])

    h = _mod1(ctx, x, nw1, ss1)
    if dist is None:
        bh4, ba4, w_o, g4, u4, dn4 = wts
        p = _mm_in(h, w_in, tmt)
        o0, st0 = _hgrn_fwd(p, lg, rev=False)
        o1, st1 = _hgrn_fwd(p, lg, rev=True)
    else:
        core, chip = dist
        half = wts[3].shape[1] // 2
        p, first = _mm_in(h, w_in, tmt, carry=_carry_join(_carry_gather(list(wts[0:3])),
                                                          _carry_gather([wts[3]], rows=[(0, half)])))
        (o0, st0), (g8,) = _hgrn_fwd(p, lg, rev=False, carry=_carry_gather([first[3]], rows=[(half, half)]))
        o1, st1 = _hgrn_fwd(p, lg, rev=True)
        bh4, ba4, w_o, g4 = four(first[0]), four(first[1]), four(first[2]).reshape(D, D), four(g8)
    y_hg = _readout(o0, o1, p, hw4)
    qr, k4, v4 = _qk_prep(p, cos, sin, qnw8, knw2, bd512, bd128, dup)
    if dist is None:
        y_at, lse = _attn_fwd(qr, k4, v4, sinks)
    else:
        (y_at, lse), (u8,) = _attn_fwd(qr, k4, v4, sinks, carry=_carry_gather([wts[4]]))
        u4 = four(u8)
    ah = _mm_cs(y_hg, bh4, name="mm_bh")
    aa = _mm_cs(y_at, ba4, name="mm_ba")
    mixed = _merge(ah, aa, p)
    ao, x1, h2 = _out_proj_mod2(mixed, w_o, x, g1, nw2, ss2)
    if dist is None:
        a4, b4, z4 = _ffn_up(h2, g4, u4)
    else:
        (a4, b4, z4), (dn8,) = _ffn_up(h2, g4, u4, carry=_carry_gather([wts[5]]))
        dn4 = four(dn8)
    sq, dx2, dyb, dg2 = _ffn_down_loss(z4, dn4, x1, g2, tgt)

    da4, db4 = _ffn_dz(dyb, dn4, a4, b4)
    g_dn = _ffn_gdn(z4, dyb)
    if dist is None:
        dh2 = _ffn_dh2(da4, db4, g4, u4)
    else:
        dn_units = [halves(g_dn)]
        dh2, dn_recv = _ffn_dh2(da4, db4, g4, u4, carry=_carry_pairx(dn_units))
        dn_pairs = _rs_pair_add(dn_units, dn_recv, core)
    if dist is None:
        g_g, g_u = _ffn_ggu(h2, da4, db4)
    else:
        (g_g, g_u), c_dn = _ffn_ggu(h2, da4, db4, carry=_carry_chipx(dn_pairs))
        red_dn = _rs_chip_add(dn_pairs, c_dn, core, chip)
    dx1, dattn, dss2, dnw2, dg1 = _mod2_bwd(x1, dh2, dx2, ao, nw2, ss2, g1)
    dm = _mm(dattn, w_o, name="mm_dm", mode="nt", tm=512, tn=D, tk=D)
    g_o = _mm(mixed, dattn, name="mm_go", mode="tn", out_dtype=BF16, tm=D, tn=D, tk=512)
    if dist is None:
        dp, dmh, dma = _merge_bwd(dm, ah, aa, p)
    else:
        gu_units = [halves(g_g), halves(g_u)]
        (dp, dmh, dma), gu_recv = _merge_bwd(dm, ah, aa, p, carry=_carry_pairx(gu_units))
        ffn_pairs = list(dn_pairs) + list(_rs_pair_add(gu_units, gu_recv, core))
    dy_hg = _mm_cs_nt(dmh, bh4, name="mm_dyh")
    dy_at = _mm_cs_nt(dma, ba4, name="mm_dya")
    g_bh = _mm_cs_tn(y_hg, dmh, D // 4, name="mm_gbh")
    g_ba = _mm_cs_tn(y_at, dma, D // 4, name="mm_gba")
    if dist is None:
        dp, do, dhw4 = _readout_bwd(o0, o1, p, hw4, dy_hg, dp)
        dq, dkw, dvw, dkc, dvc, dsk = _attn_bwd(qr, k4, v4, sinks, y_at, lse, dy_at)
        dp, dqnw8, dknw2 = _attn_post(p, cos, sin, qnw8, knw2, bd512, bd128, dupt, dq, dkw, dvw, dkc, dvc, dp)
    else:
        mix_units = [halves(g_bh), halves(g_ba), halves(g_o.reshape(4, D // 4, D))]
        (dp, do, dhw4), mix_recv = _readout_bwd(o0, o1, p, hw4, dy_hg, dp, carry=_carry_pairx(mix_units))
        mix_pairs = _rs_pair_add(mix_units, mix_recv, core)
        (dq, dkw, dvw, dkc, dvc, dsk), bwd = _attn_bwd(
            qr, k4, v4, sinks, y_at, lse, dy_at,
            carry=_carry_join(_carry_chipx(ffn_pairs[1:2]), _carry_sibx(red_dn)))
        red_g = _rs_chip_add(ffn_pairs[1:2], bwd[0:1], core, chip)
        (dp, dqnw8, dknw2), post = _attn_post(
            p, cos, sin, qnw8, knw2, bd512, bd128, dupt, dq, dkw, dvw, dkc, dvc, dp,
            carry=_carry_join(_carry_chipx(ffn_pairs[2:3]), _carry_sibx(red_g)))
        red_u = _rs_chip_add(ffn_pairs[2:3], post[0:1], core, chip)
    if dist is None:
        dp, dv0, dq0, dlg0 = _hgrn_bwd(p, lg, do, st0, dp, None, rev=False)
        dp, dlg1 = _hgrn_bwd(p, lg, do, st1, dp, (dv0, dq0), rev=True)
    else:
        (dp, dv0, dq0, dlg0), mid = _hgrn_bwd(p, lg, do, st0, dp, None, rev=False,
                                              carry=_carry_join(_carry_chipx(mix_pairs), _carry_sibx(red_u)))
        mix_reds = _rs_chip_add(mix_pairs, mid[0:3], core, chip)
        (dp, dlg1), mix_done = _hgrn_bwd(p, lg, do, st1, dp, (dv0, dq0), rev=True, carry=_carry_sibx(mix_reds))
        ffn_done = bwd[1:2] + post[1:2] + mid[3:4]
    g_in = _mm_gin(dp, h, tmt)
    if dist is None:
        dh = _mm_dh(dp, w_in, tmt)
        gx, dss1, dnw1 = _mod1_bwd(ctx, x, dh, dx1, nw1, ss1)
        rs = None
    else:
        in_units = [halves(g_in.reshape(4, NCOL // 4, D))]
        dh, in_recv = _mm_dh(dp, w_in, tmt, carry=_carry_pairx(in_units))
        in_pairs = _rs_pair_add(in_units, in_recv, core)
        first_rows = (0, in_pairs[0].shape[1] // 2)
        (gx, dss1, dnw1), in_part = _mod1_bwd(ctx, x, dh, dx1, nw1, ss1,
                                              carry=_carry_chipx(in_pairs, rows=first_rows))
        rs = dict(ffn_done=ffn_done, mix_done=mix_done, in_pairs=in_pairs, in_part=in_part)

    dmod = jnp.concatenate([dss1[1], dg1, dss2, dg2], axis=0)
    dmodc = dss1[0]
    raw = (dss1, dg1, dss2, dg2, dnw1, dnw2, dhw4, dqnw8, dknw2, dsk, dlg0, dlg1)
    small = dict(raw=raw, dmod=dmod, dmodc=dmodc, dnw1=dnw1, dnw2=dnw2,
                 dhw=dhw4.reshape(4, HGD).sum(0, keepdims=True),
                 dqnw=dqnw8.reshape(8, HDIM).sum(0, keepdims=True),
                 dknw=dknw2.reshape(2, HDIM).sum(0, keepdims=True),
                 dsinks=dsk[:, 0], dlg=jnp.concatenate([dlg0, dlg1], axis=0))
    big = dict(w_in=g_in, w_bh=g_bh, w_ba=g_ba, w_o=g_o, w_g=g_g, w_u=g_u, w_dn=g_dn)
    return sq, gx, big, small, rs


def _place():
    x, y, c = lax.axis_index("x"), lax.axis_index("y"), lax.axis_index("c")
    return x, y, c


def _gather_blocks(x_refs, out_refs, send_sems, recv_sems, local_sems):
    n = len(out_refs)
    x, y, c = _place()
    me, sibling = (x, y, c), (x, y, 1 - c)
    chips = [(1 - x, y), (x, 1 - y), (1 - x, 1 - y)]

    def slot(u, px, py, pc):
        return out_refs[u].at[4 * px + 2 * py + pc]

    def copy(u, k, block, to, src=None):
        return pltpu.make_async_remote_copy(
            src_ref=slot(u, *block) if src is None else src, dst_ref=slot(u, *block),
            send_sem=send_sems.at[u, k], recv_sem=recv_sems.at[u, k], device_id=to, device_id_type=MESH)

    mines = []
    if x_refs is not None:
        mines = [pltpu.make_async_copy(x_refs[u], slot(u, *me), local_sems.at[u]) for u in range(n)]
    for cp in mines:
        cp.start()
    first = []
    for u in range(n):
        src = None if x_refs is None else x_refs[u]
        first.append(copy(u, 0, me, sibling, src=src))
        first += [copy(u, 1 + j, me, (*chip, c), src=src) for j, chip in enumerate(chips)]
    for cp in first:
        cp.start()
    passed = []
    for j, chip in enumerate(chips):
        for u in range(n):
            copy(u, 1 + j, (*chip, c), me).wait_recv()
            fwd = copy(u, 4 + j, (*chip, c), sibling)
            fwd.start()
            passed.append(fwd)
    for u in range(n):
        copy(u, 0, sibling, me).wait_recv()
    for j, chip in enumerate(chips):
        for u in range(n):
            copy(u, 4 + j, (*chip, 1 - c), me).wait_recv()
    for cp in first + passed:
        cp.wait_send()
    for cp in mines:
        cp.wait()


def _gather_sems(n):
    return [pltpu.SemaphoreType.DMA((n, 7)), pltpu.SemaphoreType.DMA((n, 7)), pltpu.SemaphoreType.DMA((n,))]


def _allgather(blks, *, name, in_vmem):
    n = len(blks)
    space = pltpu.VMEM if in_vmem else pl.ANY

    def body(*refs):
        _gather_blocks(refs[:n], refs[n:2 * n], *refs[2 * n:])

    return pl.pallas_call(
        body, name=name, out_shape=[jax.ShapeDtypeStruct((8,) + b.shape, b.dtype) for b in blks],
        in_specs=[pl.BlockSpec(memory_space=space)] * n, out_specs=[pl.BlockSpec(memory_space=space)] * n,
        scratch_shapes=_gather_sems(n))(*blks)


def _cast_place(ws, c, dev):
    n = len(ws)

    def body(s_ref, *refs):
        for u in range(n):
            refs[n + u][0] = refs[u][...].astype(BF16)

    in_specs, out_specs, out_shape = [], [], []
    for w in ws:
        q, cols = w.shape[0] // 4, w.shape[1]
        in_specs.append(pl.BlockSpec((q, cols), lambda i, s: (2 * s[0] + i, 0)))
        out_specs.append(pl.BlockSpec((1, q, cols), lambda i, s: (s[1], i, 0)))
        out_shape.append(jax.ShapeDtypeStruct((8, 2 * q, cols), BF16))
    return pl.pallas_call(
        body, name="cast_place",
        grid_spec=pltpu.PrefetchScalarGridSpec(num_scalar_prefetch=1, grid=(2,), in_specs=in_specs,
                                               out_specs=out_specs),
        out_shape=_out_hbm(out_shape),
        compiler_params=pltpu.CompilerParams(vmem_limit_bytes=48 << 20))(jnp.stack([c, dev]), *_in_hbm(ws))


def _gather_phases(out_refs, send_sems, recv_sems, rows=None):
    n = len(out_refs)
    x, y, c = _place()
    me, sibling = (x, y, c), (x, y, 1 - c)
    chips = [(1 - x, y), (x, 1 - y), (1 - x, 1 - y)]

    def copy(u, k, block, to):
        px, py, pc = block
        ref = out_refs[u].at[4 * px + 2 * py + pc]
        if rows is not None and rows[u] is not None:
            ref = ref.at[pl.ds(rows[u][0], rows[u][1])]
        return pltpu.make_async_remote_copy(src_ref=ref, dst_ref=ref, send_sem=send_sems.at[u, k],
                                            recv_sem=recv_sems.at[u, k], device_id=to, device_id_type=MESH)

    def start():
        for u in range(n):
            copy(u, 0, me, sibling).start()
            for j, chip in enumerate(chips):
                copy(u, 1 + j, me, (*chip, c)).start()

    def mid():
        for j, chip in enumerate(chips):
            for u in range(n):
                copy(u, 1 + j, (*chip, c), me).wait_recv()
                copy(u, 4 + j, (*chip, c), sibling).start()

    def end():
        for u in range(n):
            copy(u, 0, sibling, me).wait_recv()
        for j, chip in enumerate(chips):
            for u in range(n):
                copy(u, 4 + j, (*chip, 1 - c), me).wait_recv()
        for u in range(n):
            copy(u, 0, me, sibling).wait_send()
            for j, chip in enumerate(chips):
                copy(u, 1 + j, me, (*chip, c)).wait_send()
                copy(u, 4 + j, (*chip, c), sibling).wait_send()

    return start, mid, end


def _carry_gather(bufs, rows=None):
    n = len(bufs)
    return _Carry(bufs, [jax.ShapeDtypeStruct(b.shape, b.dtype) for b in bufs], {u: u for u in range(n)},
                  [pltpu.SemaphoreType.DMA((n, 7)), pltpu.SemaphoreType.DMA((n, 7))],
                  lambda ins, outs, sems: _gather_phases(outs, *sems, rows=rows))


def _allgather_inplace(bufs, *, name):
    n = len(bufs)

    def body(*refs):
        for phase in _gather_phases(refs[n:2 * n], *refs[2 * n:]):
            phase()

    return pl.pallas_call(
        body, name=name, out_shape=[jax.ShapeDtypeStruct(b.shape, b.dtype) for b in bufs],
        in_specs=[ANY] * n, out_specs=[ANY] * n, input_output_aliases={u: u for u in range(n)},
        scratch_shapes=[pltpu.SemaphoreType.DMA((n, 7)), pltpu.SemaphoreType.DMA((n, 7))])(*bufs)


def _ag_small(raw):
    def body(dss1, dg1, dss2, dg2, dnw1, dnw2, dhw4, dqnw8, dknw2, dsk, dlg0, dlg1,
             out_ref, tot_ref, blk, send_sems, recv_sems, local_sems):
        blk[...] = jnp.zeros_like(blk)
        blk[0:2, :] = dss1[1]
        blk[2:3, :] = dg1[...]
        blk[3:5, :] = dss2[...]
        blk[5:6, :] = dg2[...]
        blk[6:8, :] = dss1[0]
        blk[8:9, :] = dnw1[...]
        blk[9:10, :] = dnw2[...]
        blk[10:11, 0:HGW] = dhw4[...]
        blk[10:11, HGW:D] = dqnw8[...]
        blk[11:12, 0:128] = dknw2[...]
        blk[12:14, 0:HGW] = dlg0[0]
        blk[14:16, 0:HGW] = dlg1[0]
        blk[16:24, 0:128] = dsk[...]
        _gather_blocks([blk], [out_ref], send_sems, recv_sems, local_sems)
        acc = out_ref[0]
        for i in range(1, 8):
            acc = acc + out_ref[i]
        tot_ref[...] = acc

    vm = pl.BlockSpec(memory_space=pltpu.VMEM)
    return pl.pallas_call(
        body, name="ag_small",
        out_shape=[jax.ShapeDtypeStruct((8, 24, D), F32), jax.ShapeDtypeStruct((24, D), F32)],
        in_specs=[vm] * 12, out_specs=[vm, vm],
        scratch_shapes=[pltpu.VMEM((24, D), F32)] + _gather_sems(1))(*raw)


def _rs_pair_exchange(units):
    n = len(units)

    def body(*refs):
        start, _, end = _pairx_phases(refs[:n], refs[n:2 * n], *refs[2 * n:])
        start()
        end()

    return pl.pallas_call(
        body, name="rs_pair_exchange", out_shape=_pairx_shapes(units),
        in_specs=[ANY] * n, out_specs=[ANY] * n,
        scratch_shapes=[pltpu.SemaphoreType.DMA((n, 4)), pltpu.SemaphoreType.DMA((n, 4))])(*units)


def _pairx_shapes(units):
    return [jax.ShapeDtypeStruct((4,) + g.shape[2:], g.dtype) for g in units]


def _pairx_phases(g_refs, r_refs, send_sems, recv_sems):
    n = len(g_refs)
    x, y, c = _place()
    cps = [pltpu.make_async_remote_copy(
        src_ref=g_refs[u].at[j, 1 - c], dst_ref=r_refs[u].at[j], send_sem=send_sems.at[u, j],
        recv_sem=recv_sems.at[u, j], device_id=(x, y, 1 - c), device_id_type=MESH)
        for u in range(n) for j in range(4)]

    def start():
        for cp in cps:
            cp.start()

    def end():
        for cp in cps:
            cp.wait()

    return start, None, end


def _carry_pairx(units):
    n = len(units)
    return _Carry(units, _pairx_shapes(units), {},
                  [pltpu.SemaphoreType.DMA((n, 4)), pltpu.SemaphoreType.DMA((n, 4))],
                  lambda ins, outs, sems: _pairx_phases(ins, outs, *sems))


def _rs_pair_add(units, recvs, c):
    n = len(units)

    def body(c_ref, *refs):
        for u in range(n):
            refs[2 * n + u][...] = (refs[u][0].astype(F32) + refs[n + u][...].astype(F32)).astype(BF16)

    in_specs, out_specs, out_shape = [], [], []
    for g in units:
        h, w = g.shape[2] // 2, g.shape[3]
        in_specs.append(pl.BlockSpec((1, 1, h, w), lambda j, i, cr: (j, cr[0], i, 0)))
    for g in units:
        h, w = g.shape[2] // 2, g.shape[3]
        in_specs.append(pl.BlockSpec((1, h, w), lambda j, i, cr: (j, i, 0)))
        out_specs.append(pl.BlockSpec((1, h, w), lambda j, i, cr: (j, i, 0)))
        out_shape.append(jax.ShapeDtypeStruct((4, 2 * h, w), BF16))
    return pl.pallas_call(
        body, name="rs_pair_add",
        grid_spec=pltpu.PrefetchScalarGridSpec(num_scalar_prefetch=1, grid=(4, 2), in_specs=in_specs,
                                               out_specs=out_specs),
        out_shape=_out_hbm(out_shape),
        compiler_params=pltpu.CompilerParams(vmem_limit_bytes=48 << 20))(
            c.reshape(1), *_in_hbm(list(units) + list(recvs)))


def _rs_chip_exchange(pairs):
    n = len(pairs)

    def body(*refs):
        start, _, end = _chipx_phases(refs[:n], refs[n:2 * n], *refs[2 * n:])
        start()
        end()

    return pl.pallas_call(
        body, name="rs_chip_exchange", out_shape=[jax.ShapeDtypeStruct(p.shape, p.dtype) for p in pairs],
        in_specs=[ANY] * n, out_specs=[ANY] * n,
        scratch_shapes=[pltpu.SemaphoreType.DMA((n, 3)), pltpu.SemaphoreType.DMA((n, 3))])(*pairs)


def _chipx_phases(p_refs, r_refs, send_sems, recv_sems, rows=None):
    n = len(p_refs)
    x, y, c = _place()
    k = 2 * x + y

    def part(ref):
        return ref if rows is None else ref.at[pl.ds(rows[0], rows[1])]

    sends = []
    for d in range(1, 4):
        j = (k + d) % 4
        for u in range(n):
            sends.append(pltpu.make_async_remote_copy(
                src_ref=part(p_refs[u].at[j]), dst_ref=part(r_refs[u].at[k]), send_sem=send_sems.at[u, d - 1],
                recv_sem=recv_sems.at[u, d - 1], device_id=(j // 2, j % 2, c), device_id_type=MESH))

    def start():
        for cp in sends:
            cp.start()

    def end():
        for d in range(1, 4):
            src = (k + 4 - d) % 4
            for u in range(n):
                pltpu.make_async_remote_copy(
                    src_ref=part(p_refs[u].at[src]), dst_ref=part(r_refs[u].at[src]),
                    send_sem=send_sems.at[u, d - 1], recv_sem=recv_sems.at[u, d - 1], device_id=(x, y, c),
                    device_id_type=MESH).wait_recv()
        for cp in sends:
            cp.wait_send()

    return start, None, end


def _carry_chipx(pairs, rows=None, into=None):
    n = len(pairs)
    sems = [pltpu.SemaphoreType.DMA((n, 3)), pltpu.SemaphoreType.DMA((n, 3))]
    shapes = [jax.ShapeDtypeStruct(p.shape, p.dtype) for p in pairs]
    if into is None:
        return _Carry(pairs, shapes, {}, sems, lambda ins, outs, s: _chipx_phases(ins, outs, *s, rows=rows))
    return _Carry(list(pairs) + list(into), shapes, {n + u: u for u in range(n)}, sems,
                  lambda ins, outs, s: _chipx_phases(ins[:n], outs, *s, rows=rows))


def _rs_chip_add(pairs, contribs, c, chip):
    n = len(pairs)

    def body(s_ref, *refs):
        for u in range(n):
            a, b, c_, d = refs[4 * u:4 * u + 4]
            refs[4 * n + u][0] = ((a[0].astype(F32) + b[0].astype(F32)) + c_[0].astype(F32)) + d[0].astype(F32)

    in_specs, out_specs, out_shape, args = [], [], [], []
    for p, r in zip(pairs, contribs):
        h, w = p.shape[1] // 2, p.shape[2]
        in_specs += [pl.BlockSpec((1, h, w), functools.partial(lambda d, i, s: ((s[1] + d) % 4, i, 0), d))
                     for d in range(4)]
        args += [p, r, r, r]
        out_specs.append(pl.BlockSpec((1, h, w), lambda i, s: (s[0], i, 0)))
        out_shape.append(jax.ShapeDtypeStruct((2, 2 * h, w), F32))
    return pl.pallas_call(
        body, name="rs_chip_add",
        grid_spec=pltpu.PrefetchScalarGridSpec(num_scalar_prefetch=1, grid=(2,), in_specs=in_specs,
                                               out_specs=out_specs),
        out_shape=_out_hbm(out_shape),
        compiler_params=pltpu.CompilerParams(vmem_limit_bytes=48 << 20))(jnp.stack([c, chip]), *_in_hbm(args))


def _rs_sibling_gather(reds):
    n = len(reds)

    def body(*refs):
        start, _, end = _sibx_phases(refs[n:2 * n], *refs[2 * n:])
        start()
        end()

    return pl.pallas_call(
        body, name="rs_sibling_gather", out_shape=[jax.ShapeDtypeStruct(r.shape, r.dtype) for r in reds],
        in_specs=[ANY] * n, out_specs=[ANY] * n, input_output_aliases={u: u for u in range(n)},
        scratch_shapes=[pltpu.SemaphoreType.DMA((n,))] * 2)(*reds)


def _sibx_phases(o_refs, send_sems, recv_sems):
    n = len(o_refs)
    x, y, c = _place()
    cps = [pltpu.make_async_remote_copy(
        src_ref=o_refs[u].at[c], dst_ref=o_refs[u].at[c], send_sem=send_sems.at[u], recv_sem=recv_sems.at[u],
        device_id=(x, y, 1 - c), device_id_type=MESH) for u in range(n)]

    def start():
        for cp in cps:
            cp.start()

    def end():
        for u in range(n):
            cps[u].wait_send()
            pltpu.make_async_remote_copy(
                src_ref=o_refs[u].at[1 - c], dst_ref=o_refs[u].at[1 - c], send_sem=send_sems.at[u],
                recv_sem=recv_sems.at[u], device_id=(x, y, 1 - c), device_id_type=MESH).wait_recv()

    return start, None, end


def _carry_sibx(reds):
    n = len(reds)
    return _Carry(reds, [jax.ShapeDtypeStruct(r.shape, r.dtype) for r in reds], {u: u for u in range(n)},
                  [pltpu.SemaphoreType.DMA((n,))] * 2, lambda ins, outs, sems: _sibx_phases(outs, *sems))


def _prologue(blk, c_ctx, w, b, in8):
    n = w.shape[1]

    def body(blk_ref, cctx_ref, w_ref, b_ref, _in_in, g0_ref, c16_ref, g1_ref, in_ref, mod_s,
             s1, r1, l1, s2, r2, l2, s3, r3):
        start, mid, end = _gather_phases([in_ref], s3, r3)
        start()
        _gather_blocks([blk_ref], [g0_ref], s1, r1, l1)
        c16 = jnp.concatenate([g0_ref[i, 0:1, :] for i in range(8)] + [cctx_ref[...], jnp.zeros((7, D), F32)],
                              axis=0)
        c16_ref[...] = c16
        mod_s[...] = _dot(c16 * _sig(c16), w_ref[...], prec=HI) + b_ref[...]
        _gather_blocks([mod_s], [g1_ref], s2, r2, l2)
        mid()
        end()

    vm = pl.BlockSpec(memory_space=pltpu.VMEM)
    return pl.pallas_call(
        body, name="prologue",
        out_shape=[jax.ShapeDtypeStruct((8, 8, D), F32), jax.ShapeDtypeStruct((16, D), F32),
                   jax.ShapeDtypeStruct((8, 16, n), F32), jax.ShapeDtypeStruct(in8.shape, in8.dtype)],
        in_specs=[vm, vm, vm, vm, ANY], out_specs=[vm, vm, vm, ANY], input_output_aliases={4: 3},
        scratch_shapes=[pltpu.VMEM((16, n), F32)] + _gather_sems(1) + _gather_sems(1)
        + [pltpu.SemaphoreType.DMA((1, 7)), pltpu.SemaphoreType.DMA((1, 7))],
        compiler_params=pltpu.CompilerParams(vmem_limit_bytes=48 << 20))(blk, c_ctx, w, b, in8)


def _ada_bwd(c16, dmod16, w):
    n = w.shape[1]
    tn = 512

    def body(c_ref, d_ref, w_ref, gw_ref, gc_ref):
        j = pl.program_id(0)

        @pl.when(j == 0)
        def _():
            gc_ref[...] = jnp.zeros_like(gc_ref)

        cc = c_ref[...]
        dm = d_ref[...]
        gw_ref[...] = _dot(cc * _sig(cc), dm, TN, prec=HI)
        gc_ref[...] += _dot(dm, w_ref[...], NT, prec=HI)

    return _pcall(body, name="ada_bwd", grid=(n // tn,),
                  in_specs=[_full((16, D)), pl.BlockSpec((16, tn), lambda j: (0, j)),
                            pl.BlockSpec((D, tn), lambda j: (0, j))],
                  out_specs=[pl.BlockSpec((D, tn), lambda j: (0, j)), _full((16, D))],
                  out_shape=[jax.ShapeDtypeStruct((D, n), F32),
                             jax.ShapeDtypeStruct((16, D), F32)])(c16, dmod16, w)


def _adam_math(w, g, m, v):
    c1 = 1.0 - ADAM_B1 ** ADAM_STEP
    c2 = 1.0 - ADAM_B2 ** ADAM_STEP
    nm = ADAM_B1 * m + (1.0 - ADAM_B1) * g
    nv = ADAM_B2 * v + (1.0 - ADAM_B2) * (g * g)
    return -ADAM_LR * ((nm / c1) / (jnp.sqrt(nv / c2) + ADAM_EPS) + ADAM_WD * w), nm, nv


def _adamw_small(ws, gs, ms, vs):
    n = len(ws)

    def body(*refs):
        for u in range(n):
            d_, nm, nv = _adam_math(refs[u][...], refs[n + u][...], refs[2 * n + u][...], refs[3 * n + u][...])
            refs[4 * n + u][...] = d_
            refs[5 * n + u][...] = nm
            refs[6 * n + u][...] = nv

    specs = [_full(w.shape) for w in ws]
    shapes = [jax.ShapeDtypeStruct(w.shape, F32) for w in ws]
    out = _pcall(body, name="adamw_small", grid=(1,), in_specs=specs * 4, out_specs=specs * 3,
                 out_shape=shapes * 3)(*ws, *gs, *ms, *vs)
    return out[:n], out[n:2 * n], out[2 * n:]


def _cctx_grad(parts, c_ctx):
    def body(p_ref, c_ref, o_ref):
        acc = p_ref[0:1, :]
        for k in range(1, 4):
            acc = acc + p_ref[k:k + 1, :]
        cc = c_ref[...]
        s = _sig(cc)
        o_ref[...] = acc * (s * (1.0 + cc * (1.0 - s)))

    return _pcall(body, name="cctx_grad", grid=(1,), in_specs=[_full(parts.shape), _full((1, D))],
                  out_specs=_full((1, D)), out_shape=jax.ShapeDtypeStruct((1, D), F32))(parts, c_ctx)


ADAM_STEPS = 8


def _adamw_multi(ws, gs, ms, vs, *, name, carry=None):
    n = len(ws)

    def body(*refs):
        for u in range(n):
            refs[4 * n + u][...], refs[5 * n + u][...], refs[6 * n + u][...] = _adam_math(
                refs[u][...], refs[n + u][...], refs[2 * n + u][...], refs[3 * n + u][...])

    specs = [pl.BlockSpec((w.shape[0] // ADAM_STEPS, w.shape[1]), lambda i: (i, 0)) for w in ws]
    shapes = [jax.ShapeDtypeStruct(w.shape, F32) for w in ws]
    res = _pcall(body, name=name, grid=(ADAM_STEPS,), in_specs=specs * 4, out_specs=specs * 3,
                 out_shape=shapes * 3, carry=carry)(*ws, *gs, *ms, *vs)
    out, extra = res if carry is not None else (res, None)
    return (out[:n], out[n:2 * n], out[2 * n:]), extra


def kernel(x, c, ctx, c_ctx, w_ada, b_ada, norm_mix_w, norm_ffn_w, w_in, hgrn_lb_logits, hgrn_norm_w, q_norm_w, k_norm_w, attn_sinks, w_branch_hgrn, w_branch_attn, w_out, w_ffn_gate, w_ffn_up, w_ffn_down, loss_target, m_c_ctx, m_w_ada, m_b_ada, m_norm_mix_w, m_norm_ffn_w, m_w_in, m_hgrn_lb_logits, m_hgrn_norm_w, m_q_norm_w, m_k_norm_w, m_attn_sinks, m_w_branch_hgrn, m_w_branch_attn, m_w_out, m_w_ffn_gate, m_w_ffn_up, m_w_ffn_down, v_c_ctx, v_w_ada, v_b_ada, v_norm_mix_w, v_norm_ffn_w, v_w_in, v_hgrn_lb_logits, v_hgrn_norm_w, v_q_norm_w, v_k_norm_w, v_attn_sinks, v_w_branch_hgrn, v_w_branch_attn, v_w_out, v_w_ffn_gate, v_w_ffn_up, v_w_ffn_down):
    xi, yi, ci = _place()
    chip = 2 * xi + yi
    dev = 2 * chip + ci
    s_len = x.shape[1]

    shards = [w_in[0].T, w_branch_hgrn[0], w_branch_attn[0], w_out[0], w_ffn_gate[0].T, w_ffn_up[0].T,
              w_ffn_down[0]]
    bufs = _cast_place(shards, ci, dev)

    lbrow = jnp.pad(hgrn_lb_logits.reshape(1, 512), ((0, 0), (0, D - 512)))
    blk = jnp.concatenate([c, lbrow, jnp.zeros((6, D), F32)], axis=0)
    nada = w_ada.shape[2]
    b_sh = lax.dynamic_slice(b_ada, (0, chip * nada), (1, nada))
    g0, c16, g1, in8 = _prologue(blk, c_ctx[None], w_ada[0], b_sh, bufs[0])
    lg = g0[0::2, 1, :512].reshape(4, 2, 2, 128).transpose(1, 2, 0, 3).reshape(2, 2, HGW)
    modall = g1[0::2].transpose(1, 0, 2).reshape(16, 4 * nada)
    mod = lax.dynamic_slice(modall, (dev, 0), (1, 6 * D)).reshape(6, D)
    modc = modall[8].reshape(6, D)[:2]

    sq, gx, _, small, rs = _local_step(
        x[0], ctx[0], loss_target[0], mod, modc, norm_mix_w, norm_ffn_w, lg, hgrn_norm_w, q_norm_w,
        k_norm_w, attn_sinks[0], in8.reshape(NCOL, D), bufs[1:], dist=(ci, chip))
    loss = lax.psum(0.5 * jnp.sum(sq) / D, ("x", "y", "c"))

    def whole(r):
        return r.reshape(2 * r.shape[1], r.shape[2])

    g_dn, g_g, g_u = [whole(r) for r in rs["ffn_done"]]
    g_bh, g_ba, g_o = [whole(r) for r in rs["mix_done"]]
    in_pairs = rs["in_pairs"]
    rest_rows = (in_pairs[0].shape[1] // 2, in_pairs[0].shape[1] // 2)

    g2, tot = _ag_small(small["raw"])
    dmodc_tot = jnp.pad(tot[6:8].reshape(1, 2 * D), ((0, 0), (0, 4 * D)))
    g_b_ada = tot[0:6].reshape(1, 6 * D) + dmodc_tot
    dmod16 = jnp.concatenate([g2[:, 0:6].reshape(8, 6 * D), dmodc_tot, jnp.zeros((7, 6 * D), F32)], axis=0)
    g_w_ada, gc_part = _ada_bwd(c16, lax.dynamic_slice(dmod16, (0, chip * nada), (16, nada)), w_ada[0])
    g3, = _allgather([gc_part[8:16]], name="ag_cctx", in_vmem=True)
    g_c_ctx = _cctx_grad(g3[0::2, 0], c_ctx[None])[0]
    g_nw1 = tot[8:9]
    g_nw2 = tot[9:10]
    g_hw = tot[10, :HGW].reshape(4, HGD).sum(0, keepdims=True)
    g_qnw = tot[10, HGW:].reshape(8, HDIM).sum(0, keepdims=True)
    g_knw = tot[11, :128].reshape(2, HDIM).sum(0, keepdims=True)
    g_sinks = tot[16:24, 0][None]
    g_lg = lax.dynamic_slice(tot[12:16, :HGW].reshape(2, 2, HGW), (0, 0, chip * 128), (2, 2, 128))

    names = ["c_ctx", "w_ada", "b_ada", "norm_mix_w", "norm_ffn_w", "w_in", "hgrn_lb_logits", "hgrn_norm_w",
             "q_norm_w", "k_norm_w", "attn_sinks", "w_branch_hgrn", "w_branch_attn", "w_out", "w_ffn_gate",
             "w_ffn_up", "w_ffn_down"]
    ws = dict(zip(names, [c_ctx, w_ada, b_ada, norm_mix_w, norm_ffn_w, w_in, hgrn_lb_logits, hgrn_norm_w,
                          q_norm_w, k_norm_w, attn_sinks, w_branch_hgrn, w_branch_attn, w_out, w_ffn_gate,
                          w_ffn_up, w_ffn_down]))
    ms = dict(zip(names, [m_c_ctx, m_w_ada, m_b_ada, m_norm_mix_w, m_norm_ffn_w, m_w_in, m_hgrn_lb_logits,
                          m_hgrn_norm_w, m_q_norm_w, m_k_norm_w, m_attn_sinks, m_w_branch_hgrn,
                          m_w_branch_attn, m_w_out, m_w_ffn_gate, m_w_ffn_up, m_w_ffn_down]))
    vs = dict(zip(names, [v_c_ctx, v_w_ada, v_b_ada, v_norm_mix_w, v_norm_ffn_w, v_w_in, v_hgrn_lb_logits,
                          v_hgrn_norm_w, v_q_norm_w, v_k_norm_w, v_attn_sinks, v_w_branch_hgrn,
                          v_w_branch_attn, v_w_out, v_w_ffn_gate, v_w_ffn_up, v_w_ffn_down]))
    transposed = ("w_in", "w_ffn_gate", "w_ffn_up")

    def view(a, n):
        return a[0].T if n in transposed else a[0]

    def unview(a, n):
        return a.T[None] if n in transposed else a[None]

    delta, new_m, new_v, grads = {}, {}, {}, {}

    def big_adamw(group, gs, name, carry=None):
        (d_, m_, v_), extra = _adamw_multi([view(ws[n], n) for n in group], gs, [view(ms[n], n) for n in group],
                                           [view(vs[n], n) for n in group], name=name, carry=carry)
        for i, n in enumerate(group):
            grads[n], delta[n], new_m[n], new_v[n] = (unview(gs[i], n), unview(d_[i], n), unview(m_[i], n),
                                                      unview(v_[i], n))
        return extra

    in_contribs = big_adamw(["w_ffn_down", "w_ffn_gate", "w_ffn_up", "w_out", "w_branch_hgrn", "w_branch_attn"],
                            [g_dn, g_g, g_u, g_o, g_bh, g_ba], "adamw_first",
                            carry=_carry_chipx(in_pairs, rows=rest_rows, into=rs["in_part"]))
    in_reds = _rs_chip_add(in_pairs, in_contribs, ci, chip)
    g_in, = [whole(r) for r in _rs_sibling_gather(in_reds)]
    big_adamw(["w_in", "w_ada"], [g_in, g_w_ada], "adamw_second")
    grads.update(c_ctx=g_c_ctx, b_ada=g_b_ada, norm_mix_w=g_nw1, norm_ffn_w=g_nw2, hgrn_lb_logits=g_lg,
                 hgrn_norm_w=g_hw, q_norm_w=g_qnw, k_norm_w=g_knw, attn_sinks=g_sinks)
    small_names = [n for n in names if n not in delta]

    def two_d(a):
        return a.reshape(1, -1) if a.ndim == 1 else a

    sd, sm_, sv = _adamw_small(*[[two_d(d[n]) for n in small_names] for d in (ws, grads, ms, vs)])
    for i, n in enumerate(small_names):
        for dst, src in ((delta, sd), (new_m, sm_), (new_v, sv)):
            dst[n] = src[i].reshape(ws[n].shape)
    return (loss, gx[None], *[grads[n] for n in names], *[delta[n] for n in names],
            *[new_m[n] for n in names], *[new_v[n] for n in names])
```

```python
import functools

import numpy as np
import jax
import jax.numpy as jnp
from jax import lax
from jax.experimental import pallas as pl
from jax.experimental.pallas import tpu as pltpu

F32 = jnp.float32
BF16 = jnp.bfloat16
HI = lax.Precision.HIGHEST
MESH = pl.DeviceIdType.MESH

D = 1024
L = 256
TM = 256
HGW = 512
HGD = 128
CH = 32
ATW = 512
HDIM = 64
BLK = 128
GRID_W = 64
DFF = 2816
NCOL = 5376
EPS = 1e-6
ROPE_THETA = 10000.0

C_FB, C_INP, C_QHG, C_FF = 0, 1, 2, 3
C_GATES = 1
C_GHG, C_QRAW = 8, 9
C_KV = 20
C_QKV = 6

ADAM_LR, ADAM_B1, ADAM_B2, ADAM_EPS, ADAM_WD, ADAM_STEP = 0.001, 0.9, 0.999, 1e-08, 0.01, 10

NN = (((1,), (0,)), ((), ()))
NT = (((1,), (1,)), ((), ()))
TN = (((0,), (0,)), ((), ()))


def _dot(a, b, dims=NN, prec=None):
    return lax.dot_general(a, b, dims, precision=prec, preferred_element_type=F32)


def _bdot(a, b, dims=NN):
    return _dot(a.astype(BF16), b.astype(BF16), dims)


def _sig(x):
    return 1.0 / (1.0 + jnp.exp(-x))


class _Carry:
    def __init__(self, ins, outs, aliases, scratch, phases):
        self.ins, self.outs, self.aliases, self.scratch, self.phases = ins, outs, aliases, scratch, phases


def _in_hbm(args):
    return [pltpu.with_memory_space_constraint(a, pltpu.HBM) for a in args]


def _out_hbm(shapes):
    if isinstance(shapes, (list, tuple)):
        return [pltpu.HBM(s.shape, s.dtype) for s in shapes]
    return pltpu.HBM(shapes.shape, shapes.dtype)


def _carry_join(a, b):
    na_in, na_out, na_sc = len(a.ins), len(a.outs), len(a.scratch)
    aliases = dict(a.aliases)
    aliases.update({na_in + i: na_out + o for i, o in b.aliases.items()})

    def phases(ins, outs, sems):
        pa = a.phases(ins[:na_in], outs[:na_out], sems[:na_sc])
        pb = b.phases(ins[na_in:], outs[na_out:], sems[na_sc:])

        def both(fa, fb):
            if fa is None and fb is None:
                return None

            def run():
                for fn in (fa, fb):
                    if fn is not None:
                        fn()
            return run

        return tuple(both(fa, fb) for fa, fb in zip(pa, pb))

    return _Carry(list(a.ins) + list(b.ins), list(a.outs) + list(b.outs), aliases,
                  list(a.scratch) + list(b.scratch), phases)


def _pcall(body, *, name, grid, in_specs, out_specs, out_shape, scratch=(), aliases=None, vmem_mb=48,
           carry=None):
    params = pltpu.CompilerParams(dimension_semantics=("arbitrary",) * len(grid),
                                  vmem_limit_bytes=vmem_mb << 20)
    if carry is None:
        plain = pl.pallas_call(
            body, name=name, grid=grid, in_specs=in_specs, out_specs=out_specs, out_shape=_out_hbm(out_shape),
            scratch_shapes=list(scratch), input_output_aliases=aliases or {}, compiler_params=params)
        return lambda *args: plain(*_in_hbm(args))
    single = not isinstance(out_shape, (list, tuple))
    out_specs_l = [out_specs] if single else list(out_specs)
    out_shape_l = [out_shape] if single else list(out_shape)
    n_in, n_out, n_sc = len(in_specs), len(out_shape_l), len(scratch)
    k_in, k_out = len(carry.ins), len(carry.outs)
    nsteps = int(np.prod(grid))
    assert nsteps >= 3

    def wrapped(*refs):
        ins, cins = refs[:n_in], refs[n_in:n_in + k_in]
        o0 = n_in + k_in
        outs, couts = refs[o0:o0 + n_out], refs[o0 + n_out:o0 + n_out + k_out]
        s0 = o0 + n_out + k_out
        sc, csc = refs[s0:s0 + n_sc], refs[s0 + n_sc:]
        step = pl.program_id(0)
        for ax in range(1, len(grid)):
            step = step * grid[ax] + pl.program_id(ax)
        start, mid, end = carry.phases(cins, couts, csc)
        pl.when(step == 0)(start)
        body(*ins, *outs, *sc)
        if mid is not None:
            pl.when(step == nsteps - 2)(mid)
        pl.when(step == nsteps - 1)(end)

    all_aliases = dict(aliases or {})
    all_aliases.update({n_in + i: n_out + o for i, o in carry.aliases.items()})
    call = pl.pallas_call(
        wrapped, name=name, grid=grid, in_specs=list(in_specs) + [ANY] * k_in,
        out_specs=out_specs_l + [ANY] * k_out, out_shape=_out_hbm(out_shape_l + list(carry.outs)),
        scratch_shapes=list(scratch) + list(carry.scratch), input_output_aliases=all_aliases,
        compiler_params=params)

    def run(*args):
        res = call(*_in_hbm(args), *carry.ins)
        core = res[:n_out]
        return (core[0] if single else list(core)), list(res[n_out:])

    return run


def _full(shape):
    nd = len(shape)
    return pl.BlockSpec(shape, lambda *_: (0,) * nd)


ANY = pl.BlockSpec(memory_space=pl.ANY)


def _mm(a, b, *, name, mode="nn", out_dtype=F32, tm, tn, tk):
    if mode == "nn":
        (m, k), (k2, n) = a.shape, b.shape
    elif mode == "nt":
        (m, k), (n, k2) = a.shape, b.shape
    else:
        (k, m), (k2, n) = a.shape, b.shape
    assert k == k2 and m % tm == 0 and n % tn == 0 and k % tk == 0, (name, a.shape, b.shape)
    nk = k // tk
    dims = {"nn": NN, "nt": NT, "tn": TN}[mode]

    def body(a_ref, b_ref, o_ref, acc):
        kk = pl.program_id(2)

        @pl.when(kk == 0)
        def _():
            acc[...] = jnp.zeros_like(acc)

        acc[...] += _bdot(a_ref[...], b_ref[...], dims)

        @pl.when(kk == nk - 1)
        def _():
            o_ref[...] = acc[...].astype(out_dtype)

    a_spec = (pl.BlockSpec((tk, tm), lambda i, j, kk: (kk, i)) if mode == "tn"
              else pl.BlockSpec((tm, tk), lambda i, j, kk: (i, kk)))
    b_spec = (pl.BlockSpec((tn, tk), lambda i, j, kk: (j, kk)) if mode == "nt"
              else pl.BlockSpec((tk, tn), lambda i, j, kk: (kk, j)))
    return _pcall(body, name=name, grid=(m // tm, n // tn, nk), in_specs=[a_spec, b_spec],
                  out_specs=pl.BlockSpec((tm, tn), lambda i, j, kk: (i, j)),
                  out_shape=jax.ShapeDtypeStruct((m, n), out_dtype),
                  scratch=[pltpu.VMEM((tm, tn), F32)])(a, b)


NT_IN = NCOL // 256


def _src_block(j):
    return j + jnp.where(j < 4, 2, jnp.where(j < 6, 3, jnp.where(j < 8, -6, jnp.where(
        j < 16, 5, jnp.where(j < 20, -7, -14)))))


def _mm_in(h, wt, tm, carry=None):
    tt = h.shape[0]

    def body(h_ref, w_ref, o_ref):
        o_ref[...] = _bdot(h_ref[...], w_ref[...], NT)

    return _pcall(body, name="mm_in", grid=(tt // tm, NT_IN),
                  in_specs=[pl.BlockSpec((tm, D), lambda i, j: (i, 0)),
                            pl.BlockSpec((256, D), lambda i, j: (_src_block(j), 0))],
                  out_specs=pl.BlockSpec((tm, 256), lambda i, j: (i, j)),
                  out_shape=jax.ShapeDtypeStruct((tt, NCOL), F32), carry=carry)(h, wt)


def _mm_dh(dp, wt, tm, carry=None):
    tt = dp.shape[0]
    per, ng = 3, NT_IN // 3

    def body(d_ref, w0, w1, w2, o_ref, acc):
        kk = pl.program_id(1)

        @pl.when(kk == 0)
        def _():
            acc[...] = jnp.zeros_like(acc)

        acc[...] += (_bdot(d_ref[:, 0:256], w0[...]) + _bdot(d_ref[:, 256:512], w1[...])
                     + _bdot(d_ref[:, 512:768], w2[...]))

        @pl.when(kk == ng - 1)
        def _():
            o_ref[...] = acc[...]

    wspecs = [pl.BlockSpec((256, D), functools.partial(lambda t, i, kk: (_src_block(per * kk + t), 0), t))
              for t in range(per)]
    return _pcall(body, name="mm_dh", grid=(tt // tm, ng),
                  in_specs=[pl.BlockSpec((tm, per * 256), lambda i, kk: (i, kk))] + wspecs,
                  out_specs=pl.BlockSpec((tm, D), lambda i, kk: (i, 0)),
                  out_shape=jax.ShapeDtypeStruct((tt, D), F32), scratch=[pltpu.VMEM((tm, D), F32)],
                  carry=carry)(dp, wt, wt, wt)


def _mm_gin(dp, h, tk):
    tt = dp.shape[0]
    nk = tt // tk

    def body(d_ref, h_ref, o_ref, acc):
        kk = pl.program_id(1)

        @pl.when(kk == 0)
        def _():
            acc[...] = jnp.zeros_like(acc)

        acc[...] += _bdot(d_ref[...], h_ref[...], TN)

        @pl.when(kk == nk - 1)
        def _():
            o_ref[...] = acc[...].astype(BF16)

    return _pcall(body, name="mm_gin", grid=(NT_IN, nk),
                  in_specs=[pl.BlockSpec((tk, 256), lambda j, kk: (kk, j)),
                            pl.BlockSpec((tk, D), lambda j, kk: (kk, 0))],
                  out_specs=pl.BlockSpec((256, D), lambda j, kk: (_src_block(j), 0)),
                  out_shape=jax.ShapeDtypeStruct((NCOL, D), BF16), scratch=[pltpu.VMEM((256, D), F32)])(dp, h)


def _tok_specs():
    assert L == TM
    return [_full((TM, D)), pl.BlockSpec((TM, D), lambda i: (jnp.maximum(i - 1, 0), 0))]


def _mod1(ctx, x, nw, ss):
    rows = L + x.shape[0]

    def body(c_ref, x_ref, nw_ref, ss_ref, h_ref):
        t = jnp.where(pl.program_id(0) == 0, c_ref[...], x_ref[...])
        r = lax.rsqrt(jnp.mean(t * t, axis=-1, keepdims=True) + EPS)
        s = ss_ref[0]
        h_ref[...] = ((t * r * nw_ref[...]) * (1.0 + s[1:2]) + s[0:1]).astype(BF16)

    return _pcall(body, name="mod1", grid=(rows // TM,),
                  in_specs=_tok_specs() + [_full((1, D)),
                                           pl.BlockSpec((1, 2, D), lambda i: (jnp.minimum(i, 1), 0, 0))],
                  out_specs=pl.BlockSpec((TM, D), lambda i: (i, 0)),
                  out_shape=jax.ShapeDtypeStruct((rows, D), BF16))(ctx, x, nw, ss)


def _norm_bwd_rows(x, dh, nw, scale):
    r = lax.rsqrt(jnp.mean(x * x, axis=-1, keepdims=True) + EPS)
    xh = x * r
    dxh = dh * ((1.0 + scale) * nw)
    dx = r * (dxh - xh * jnp.mean(dxh * xh, axis=-1, keepdims=True))
    return dx, xh


def _out_proj_mod2(mixed, w_o, x, g1, nw2, ss2):
    s_len = x.shape[0]
    tm = 512

    def body(m_ref, w_ref, x_ref, g_ref, nw_ref, ss_ref, ao_ref, x1_ref, h_ref):
        ao = _bdot(m_ref[...], w_ref[...])
        ao_ref[...] = ao
        x1 = x_ref[...] + g_ref[...] * ao
        x1_ref[...] = x1
        r = lax.rsqrt(jnp.mean(x1 * x1, axis=-1, keepdims=True) + EPS)
        s = ss_ref[0]
        h_ref[...] = ((x1 * r * nw_ref[...]) * (1.0 + s[1:2]) + s[0:1]).astype(BF16)

    row = pl.BlockSpec((tm, D), lambda i: (i, 0))
    f = jax.ShapeDtypeStruct((s_len, D), F32)
    return _pcall(body, name="out_proj_mod2", grid=(s_len // tm,),
                  in_specs=[row, _full((D, D)), row, _full((1, D)), _full((1, D)), _full((1, 2, D))],
                  out_specs=[row, row, row],
                  out_shape=[f, f, jax.ShapeDtypeStruct((s_len, D), BF16)])(mixed, w_o, x, g1, nw2, ss2)


TS = 1024


def _acc_call(body, *, name, grid, in_specs, out_specs, out_shape, acc_shapes, args, carry=None):
    return _pcall(body, name=name, grid=grid, in_specs=in_specs, out_specs=out_specs, out_shape=out_shape,
                  scratch=[pltpu.VMEM(s, F32) for s in acc_shapes], carry=carry)(*args)


def _mm_cs(a, w4, *, name):
    m, k = a.shape
    _, _, ns = w4.shape

    def body(a_ref, w_ref, o_ref):
        o_ref[...] = _bdot(a_ref[...], w_ref[0])

    return _pcall(body, name=name, grid=(m // TS, 4),
                  in_specs=[pl.BlockSpec((TS, k), lambda i, j: (i, 0)),
                            pl.BlockSpec((1, k, ns), lambda i, j: (j, 0, 0))],
                  out_specs=pl.BlockSpec((TS, ns), lambda i, j: (i, j)),
                  out_shape=jax.ShapeDtypeStruct((m, 4 * ns), F32))(a, w4)


def _mm_cs_nt(a, w4, *, name):
    m = a.shape[0]
    _, k, ns = w4.shape

    def body(a_ref, w_ref, o_ref, acc):
        j = pl.program_id(1)

        @pl.when(j == 0)
        def _():
            acc[...] = jnp.zeros_like(acc)

        acc[...] += _bdot(a_ref[...], w_ref[0], NT)

        @pl.when(j == 3)
        def _():
            o_ref[...] = acc[...]

    return _acc_call(body, name=name, grid=(m // TS, 4),
                     in_specs=[pl.BlockSpec((TS, ns), lambda i, j: (i, j)),
                               pl.BlockSpec((1, k, ns), lambda i, j: (j, 0, 0))],
                     out_specs=pl.BlockSpec((TS, k), lambda i, j: (i, 0)),
                     out_shape=jax.ShapeDtypeStruct((m, k), F32), acc_shapes=[(TS, k)], args=(a, w4))


def _mm_cs_tn(a, b, ns, *, name):
    s_len, k = a.shape
    nk = s_len // TS

    def body(a_ref, b_ref, o_ref, acc):
        t = pl.program_id(1)

        @pl.when(t == 0)
        def _():
            acc[...] = jnp.zeros_like(acc)

        acc[...] += _bdot(a_ref[...], b_ref[...], TN)

        @pl.when(t == nk - 1)
        def _():
            o_ref[0] = acc[...].astype(o_ref.dtype)

    return _acc_call(body, name=name, grid=(4, nk),
                     in_specs=[pl.BlockSpec((TS, k), lambda j, t: (t, 0)),
                               pl.BlockSpec((TS, ns), lambda j, t: (t, j))],
                     out_specs=pl.BlockSpec((1, k, ns), lambda j, t: (j, 0, 0)),
                     out_shape=jax.ShapeDtypeStruct((4, k, ns), BF16), acc_shapes=[(k, ns)], args=(a, b))


def _ffn_up(h2, g4, u4, carry=None):
    s_len = h2.shape[0]
    ns = g4.shape[1]

    def body(h_ref, g_ref, u_ref, a_ref, b_ref, z_ref):
        h = h_ref[...]
        a = _bdot(h, g_ref[0], NT)
        b = _bdot(h, u_ref[0], NT)
        a_ref[0] = a.astype(BF16)
        b_ref[0] = b.astype(BF16)
        z_ref[0] = (a * _sig(a) * b).astype(BF16)

    w = pl.BlockSpec((1, ns, D), lambda i, j: (j, 0, 0))
    o = pl.BlockSpec((1, TS, ns), lambda i, j: (j, i, 0))
    f = jax.ShapeDtypeStruct((4, s_len, ns), BF16)
    return _pcall(body, name="ffn_up", grid=(s_len // TS, 4),
                  in_specs=[pl.BlockSpec((TS, D), lambda i, j: (i, 0)), w, w], out_specs=[o, o, o],
                  out_shape=[f, f, jax.ShapeDtypeStruct((4, s_len, ns), BF16)], carry=carry)(h2, g4, u4)


def _ffn_down_loss(z4, dn4, x1, g2, tgt):
    _, s_len, ns = z4.shape

    def body(z_ref, w_ref, x1_ref, g_ref, t_ref, sq_ref, dx2_ref, dyb_ref, dg_ref, acc):
        i, j = pl.program_id(0), pl.program_id(1)

        @pl.when((i == 0) & (j == 0))
        def _():
            sq_ref[...] = jnp.zeros_like(sq_ref)
            dg_ref[...] = jnp.zeros_like(dg_ref)

        @pl.when(j == 0)
        def _():
            acc[...] = jnp.zeros_like(acc)

        acc[...] += _bdot(z_ref[0], w_ref[0])

        @pl.when(j == 3)
        def _():
            y_ = acc[...]
            g = g_ref[...]
            e = x1_ref[...] + g * y_ - t_ref[...]
            sq_ref[...] += jnp.sum(e * e, axis=0, keepdims=True)
            dx2 = e * (1.0 / D)
            dx2_ref[...] = dx2
            dyb_ref[...] = (g * dx2).astype(BF16)
            dg_ref[...] += jnp.sum(dx2 * y_, axis=0, keepdims=True)

    row = pl.BlockSpec((TS, D), lambda i, j: (i, 0))
    vec = _full((1, D))
    return _acc_call(body, name="ffn_down_loss", grid=(s_len // TS, 4),
                     in_specs=[pl.BlockSpec((1, TS, ns), lambda i, j: (j, i, 0)),
                               pl.BlockSpec((1, ns, D), lambda i, j: (j, 0, 0)), row, vec, row],
                     out_specs=[vec, row, row, vec],
                     out_shape=[jax.ShapeDtypeStruct((1, D), F32), jax.ShapeDtypeStruct((s_len, D), F32),
                                jax.ShapeDtypeStruct((s_len, D), BF16), jax.ShapeDtypeStruct((1, D), F32)],
                     acc_shapes=[(TS, D)], args=(z4, dn4, x1, g2, tgt))


def _ffn_dz(dyb, dn4, a4, b4):
    _, s_len, ns = a4.shape

    def body(dy_ref, w_ref, a_ref, b_ref, da_ref, db_ref):
        dz = _bdot(dy_ref[...], w_ref[0], NT)
        a = a_ref[0].astype(F32)
        s = _sig(a)
        da_ref[0] = (dz * b_ref[0].astype(F32) * (s * (1.0 + a * (1.0 - s)))).astype(BF16)
        db_ref[0] = (dz * (a * s)).astype(BF16)

    t = pl.BlockSpec((1, TS, ns), lambda i, j: (j, i, 0))
    o = jax.ShapeDtypeStruct((4, s_len, ns), BF16)
    return _pcall(body, name="ffn_dz", grid=(s_len // TS, 4),
                  in_specs=[pl.BlockSpec((TS, D), lambda i, j: (i, 0)),
                            pl.BlockSpec((1, ns, D), lambda i, j: (j, 0, 0)), t, t],
                  out_specs=[t, t], out_shape=[o, o])(dyb, dn4, a4, b4)


def _ffn_gdn(z4, dyb):
    _, s_len, ns = z4.shape
    nk = s_len // TS

    def body(z_ref, dy_ref, o_ref, acc):
        t = pl.program_id(1)

        @pl.when(t == 0)
        def _():
            acc[...] = jnp.zeros_like(acc)

        acc[...] += _bdot(z_ref[0], dy_ref[...], TN)

        @pl.when(t == nk - 1)
        def _():
            o_ref[0] = acc[...].astype(o_ref.dtype)

    return _acc_call(body, name="ffn_gdn", grid=(4, nk),
                     in_specs=[pl.BlockSpec((1, TS, ns), lambda j, t: (j, t, 0)),
                               pl.BlockSpec((TS, D), lambda j, t: (t, 0))],
                     out_specs=pl.BlockSpec((1, ns, D), lambda j, t: (j, 0, 0)),
                     out_shape=jax.ShapeDtypeStruct((4, ns, D), BF16), acc_shapes=[(ns, D)], args=(z4, dyb))


def _ffn_dh2(da4, db4, g4, u4, carry=None):
    _, s_len, ns = da4.shape

    def body(da_ref, db_ref, g_ref, u_ref, o_ref, acc):
        j = pl.program_id(1)

        @pl.when(j == 0)
        def _():
            acc[...] = jnp.zeros_like(acc)

        acc[...] += _bdot(da_ref[0], g_ref[0]) + _bdot(db_ref[0], u_ref[0])

        @pl.when(j == 3)
        def _():
            o_ref[...] = acc[...]

    t = pl.BlockSpec((1, TS, ns), lambda i, j: (j, i, 0))
    w = pl.BlockSpec((1, ns, D), lambda i, j: (j, 0, 0))
    return _acc_call(body, name="ffn_dh2", grid=(s_len // TS, 4), in_specs=[t, t, w, w],
                     out_specs=pl.BlockSpec((TS, D), lambda i, j: (i, 0)),
                     out_shape=jax.ShapeDtypeStruct((s_len, D), F32), acc_shapes=[(TS, D)],
                     args=(da4, db4, g4, u4), carry=carry)


def _ffn_ggu(h2, da4, db4, carry=None):
    _, s_len, ns = da4.shape
    nk = s_len // TS

    def body(h_ref, da_ref, db_ref, gg_ref, gu_ref, acc_g, acc_u):
        t = pl.program_id(1)

        @pl.when(t == 0)
        def _():
            acc_g[...] = jnp.zeros_like(acc_g)
            acc_u[...] = jnp.zeros_like(acc_u)

        h = h_ref[...]
        acc_g[...] += _bdot(da_ref[0], h, TN)
        acc_u[...] += _bdot(db_ref[0], h, TN)

        @pl.when(t == nk - 1)
        def _():
            gg_ref[0] = acc_g[...].astype(BF16)
            gu_ref[0] = acc_u[...].astype(BF16)

    d = pl.BlockSpec((1, TS, ns), lambda j, t: (j, t, 0))
    o = pl.BlockSpec((1, ns, D), lambda j, t: (j, 0, 0))
    f = jax.ShapeDtypeStruct((4, ns, D), BF16)
    return _acc_call(body, name="ffn_ggu", grid=(4, nk),
                     in_specs=[pl.BlockSpec((TS, D), lambda j, t: (t, 0)), d, d], out_specs=[o, o],
                     out_shape=[f, f], acc_shapes=[(ns, D), (ns, D)], args=(h2, da4, db4), carry=carry)


def _mod2_bwd(x1, dh2, dx2, ao, nw2, ss2, g1):
    s_len = x1.shape[0]

    def body(x1_ref, dh_ref, dx2_ref, ao_ref, nw_ref, ss_ref, g_ref,
             dx1_ref, da_ref, dss_ref, dnw_ref, dg_ref):
        i = pl.program_id(0)

        @pl.when(i == 0)
        def _():
            dss_ref[...] = jnp.zeros_like(dss_ref)
            dnw_ref[...] = jnp.zeros_like(dnw_ref)
            dg_ref[...] = jnp.zeros_like(dg_ref)

        dh = dh_ref[...]
        nw = nw_ref[...]
        scale = ss_ref[0][1:2]
        dxn, xh = _norm_bwd_rows(x1_ref[...], dh, nw, scale)
        dx1 = dx2_ref[...] + dxn
        dx1_ref[...] = dx1
        da_ref[...] = (g_ref[...] * dx1).astype(BF16)
        dg_ref[...] += jnp.sum(dx1 * ao_ref[...], axis=0, keepdims=True)
        dsh = jnp.sum(dh, axis=0, keepdims=True)
        dsc = jnp.sum(dh * xh * nw, axis=0, keepdims=True)
        dss_ref[...] += jnp.concatenate([dsh, dsc], axis=0)
        dnw_ref[...] += jnp.sum(dh * xh * (1.0 + scale), axis=0, keepdims=True)

    row = pl.BlockSpec((TM, D), lambda i: (i, 0))
    vec = _full((1, D))
    return _pcall(body, name="mod2_bwd", grid=(s_len // TM,),
                  in_specs=[row, row, row, row, vec, _full((1, 2, D)), vec],
                  out_specs=[row, row, _full((2, D)), vec, vec],
                  out_shape=[jax.ShapeDtypeStruct((s_len, D), F32), jax.ShapeDtypeStruct((s_len, D), BF16),
                             jax.ShapeDtypeStruct((2, D), F32), jax.ShapeDtypeStruct((1, D), F32),
                             jax.ShapeDtypeStruct((1, D), F32)])(x1, dh2, dx2, ao, nw2, ss2, g1)


def _mod1_bwd(ctx, x, dh, dx1, nw1, ss1, carry=None):
    s_len = dx1.shape[0]
    tt = L + s_len

    def body(c_ref, x_ref, dh_ref, dx1_ref, nw_ref, ss_ref, dx_ref, dss_ref, dnw_ref):
        i = pl.program_id(0)
        tok = jnp.where(i == 0, c_ref[...], x_ref[...])

        @pl.when(i == 0)
        def _():
            dnw_ref[...] = jnp.zeros_like(dnw_ref)

        @pl.when(i <= 1)
        def _():
            dss_ref[...] = jnp.zeros_like(dss_ref)

        dh_ = dh_ref[...]
        nw = nw_ref[...]
        scale = ss_ref[0][1:2]
        dxn, xh = _norm_bwd_rows(tok, dh_, nw, scale)

        @pl.when(i >= 1)
        def _():
            dx_ref[...] = dx1_ref[...] + dxn

        dsh = jnp.sum(dh_, axis=0, keepdims=True)
        dsc = jnp.sum(dh_ * xh * nw, axis=0, keepdims=True)
        dss_ref[...] += jnp.concatenate([dsh, dsc], axis=0)[None]
        dnw_ref[...] += jnp.sum(dh_ * xh * (1.0 + scale), axis=0, keepdims=True)

    row = pl.BlockSpec((TM, D), lambda i: (i, 0))
    lat = pl.BlockSpec((TM, D), lambda i: (jnp.maximum(i - 1, 0), 0))
    sel = pl.BlockSpec((1, 2, D), lambda i: (jnp.minimum(i, 1), 0, 0))
    return _pcall(body, name="mod1_bwd", grid=(tt // TM,),
                  in_specs=_tok_specs() + [row, lat, _full((1, D)), sel],
                  out_specs=[lat, sel, _full((1, D))],
                  out_shape=[jax.ShapeDtypeStruct((s_len, D), F32), jax.ShapeDtypeStruct((2, 2, D), F32),
                             jax.ShapeDtypeStruct((1, D), F32)], carry=carry)(ctx, x, dh, dx1, nw1, ss1)


def _rows(c):
    return slice(c * CH, (c + 1) * CH)


def _chunk_masks(rev, transpose=False):
    r = lax.broadcasted_iota(jnp.int32, (TM, TM), 0)
    c = lax.broadcasted_iota(jnp.int32, (TM, TM), 1)
    same = (r // CH) == (c // CH)
    before = (c >= r) if (rev != transpose) else (c <= r)
    return same & before, same


def _chunk_scan(x, rev, transpose=False):
    r = lax.broadcasted_iota(jnp.int32, (CH, CH), 0)
    c = lax.broadcasted_iota(jnp.int32, (CH, CH), 1)
    tri = ((c >= r) if (rev != transpose) else (c <= r)).astype(F32)
    return jnp.concatenate([_dot(tri, x[_rows(ch)], prec=HI) for ch in range(x.shape[0] // CH)], axis=0)


def _chunk_total(x):
    return jnp.concatenate([jnp.broadcast_to(jnp.sum(x[_rows(ch)], axis=0, keepdims=True), (CH, x.shape[1]))
                            for ch in range(x.shape[0] // CH)], axis=0)


def _hgrn_gate(fl, qraw, lg):
    lb = 1.0 / (1.0 + jnp.exp(lg[1:2] - lg[0:1]))
    sg = _sig(fl)
    f = lb + (1.0 - lb) * sg
    q = qraw * _sig(qraw) * (HGD ** -0.5)
    return lb, sg, f, q


def _hgrn_fwd(p, lg, *, rev, carry=None):
    tt = p.shape[0]
    nt = tt // TM
    ncht = TM // CH
    d = 1 if rev else 0

    def tile_of(s):
        return jnp.where(s == 0, 0, nt - s) if rev else s

    def body(f_ref, inp_ref, q_ref, lg_ref, o_ref, st_ref, state):
        s = pl.program_id(0)

        @pl.when(s == 0)
        def _():
            state[...] = jnp.zeros_like(state)

        _, _, f, q = _hgrn_gate(f_ref[...], q_ref[...], lg_ref[0])
        lf = jnp.log(f)
        causal, _ = _chunk_masks(rev)
        cum = _chunk_scan(lf, rev)
        tot = _chunk_total(lf)
        qd = (q * jnp.exp(cum)).astype(BF16)
        kd = ((1.0 - f) * jnp.exp(-cum)).astype(BF16)
        ke = ((1.0 - f) * jnp.exp(tot - cum)).astype(BF16)
        et = jnp.exp(tot)
        v = inp_ref[...].astype(BF16)
        order = range(ncht - 1, -1, -1) if rev else range(ncht)
        outs = []
        for h in range(4):
            sl = slice(h * HGD, (h + 1) * HGD)
            qd_, kd_, ke_, v_ = qd[:, sl], kd[:, sl], ke[:, sl], v[:, sl]
            pm = jnp.where(causal, _dot(qd_, kd_, NT), 0.0).astype(BF16)
            o_h = _dot(pm, v_)
            upd = [_dot(v_[_rows(c)], ke_[_rows(c)], TN) for c in range(ncht)]
            st = state[h]
            for c in order:
                st_ref[c, h] = st
                st = st * et[c * CH:c * CH + 1, sl] + upd[c]
            state[h] = st
            inter = [_dot(qd_[_rows(c)], st_ref[c, h].astype(BF16), NT) for c in range(ncht)]
            outs.append(o_h + jnp.concatenate(inter, axis=0))
        o_ref[...] = jnp.concatenate(outs, axis=1)

    def col(cb):
        return pl.BlockSpec((TM, HGW), lambda s: (tile_of(s), cb))

    return _pcall(
        body, name="hgrn_fwd_rev" if rev else "hgrn_fwd", grid=(nt,),
        in_specs=[col(C_FB if rev else C_FF), col(C_INP), col(C_QHG),
                  pl.BlockSpec((1, 2, HGW), lambda s: (d, 0, 0))],
        out_specs=[pl.BlockSpec((TM, HGW), lambda s: (tile_of(s), 0)),
                   pl.BlockSpec((ncht, 4, HGD, HGD), lambda s: (tile_of(s), 0, 0, 0))],
        out_shape=[jax.ShapeDtypeStruct((tt, HGW), F32),
                   jax.ShapeDtypeStruct((nt * ncht, 4, HGD, HGD), F32)],
        scratch=[pltpu.VMEM((4, HGD, HGD), F32)], carry=carry)(p, p, p, lg)


def _hgrn_bwd(p, lg, do, st, dp, prev, *, rev, carry=None):
    tt = p.shape[0]
    nt = tt // TM
    ncht = TM // CH
    d = 1 if rev else 0
    second = prev is not None

    def tile_of(s):
        return jnp.where(s == nt - 1, 0, s + 1) if rev else nt - 1 - s

    def body(*refs):
        if second:
            (f_ref, inp_ref, q_ref, lg_ref, do_ref, st_ref, dvp_ref, dqp_ref, _dp_in,
             dp_ref, dlg_ref, dstate) = refs
        else:
            (f_ref, inp_ref, q_ref, lg_ref, do_ref, st_ref, _dp_in,
             dp_ref, dv_ref, dq_ref, dlg_ref, dstate) = refs
        s = pl.program_id(0)
        tile = tile_of(s)

        @pl.when(s == 0)
        def _():
            dstate[...] = jnp.zeros_like(dstate)
            dlg_ref[...] = jnp.zeros_like(dlg_ref)

        qraw = q_ref[...]
        lb, sg, f, q = _hgrn_gate(f_ref[...], qraw, lg_ref[0])
        lf = jnp.log(f)
        causal, _ = _chunk_masks(rev)
        causal_t, _ = _chunk_masks(rev, transpose=True)
        cum = _chunk_scan(lf, rev)
        tot = _chunk_total(lf)
        ea, eb, ee, et = jnp.exp(cum), jnp.exp(-cum), jnp.exp(tot - cum), jnp.exp(tot)
        qdf, kdf, kef = q * ea, (1.0 - f) * eb, (1.0 - f) * ee
        qd, kd, ke = qdf.astype(BF16), kdf.astype(BF16), kef.astype(BF16)
        v = inp_ref[...].astype(BF16)
        dob = jnp.where(tile == 0, 0.0, do_ref[...]).astype(BF16)
        order = range(ncht) if rev else range(ncht - 1, -1, -1)
        dq_l, dk_l, dv_l, dcum_l, dtot_l = [], [], [], [], []
        for h in range(4):
            sl = slice(h * HGD, (h + 1) * HGD)
            qd_, kd_, ke_, v_, do_ = qd[:, sl], kd[:, sl], ke[:, sl], v[:, sl], dob[:, sl]
            pmt = jnp.where(causal_t, _dot(kd_, qd_, NT), 0.0).astype(BF16)
            dpm = jnp.where(causal, _dot(do_, v_, NT), 0.0).astype(BF16)
            dpmt = jnp.where(causal_t, _dot(v_, do_, NT), 0.0).astype(BF16)
            dv = _dot(pmt, do_)
            dqd = _dot(dpm, kd_)
            dkd = _dot(dpmt, qd_)
            upd = [_dot(do_[_rows(c)], qd_[_rows(c)], TN) for c in range(ncht)]
            ds = dstate[h]
            ds1 = [None] * ncht
            for c in order:
                ds1[c] = ds
                ds = ds * et[c * CH:c * CH + 1, sl] + upd[c]
            dstate[h] = ds
            dke_c, dv_c, dqd_c, dtot_c = [], [], [], []
            for c in range(ncht):
                st0 = st_ref[c, h]
                dsb = ds1[c].astype(BF16)
                dke_ = _dot(v_[_rows(c)], dsb)
                dke_c.append(dke_)
                dv_c.append(_dot(ke_[_rows(c)], dsb, NT))
                dqd_c.append(_dot(do_[_rows(c)], st0.astype(BF16)))
                dt = (jnp.sum(ds1[c] * st0, axis=0, keepdims=True) * et[c * CH:c * CH + 1, sl]
                      + jnp.sum(dke_ * kef[_rows(c), sl], axis=0, keepdims=True))
                dtot_c.append(jnp.broadcast_to(dt, (CH, HGD)))
            dke = jnp.concatenate(dke_c, axis=0)
            dqd = dqd + jnp.concatenate(dqd_c, axis=0)
            dv_l.append(dv + jnp.concatenate(dv_c, axis=0))
            dtot_l.append(jnp.concatenate(dtot_c, axis=0))
            dq_l.append(dqd * ea[:, sl])
            dk_l.append(dkd * eb[:, sl] + dke * ee[:, sl])
            dcum_l.append(dqd * qdf[:, sl] - dkd * kdf[:, sl] - dke * kef[:, sl])
        dcum = jnp.concatenate(dcum_l, axis=1)
        dlf = _chunk_scan(dcum, rev, transpose=True) + jnp.concatenate(dtot_l, axis=1)
        dq_t = jnp.concatenate(dq_l, axis=1)
        dv_t = jnp.concatenate(dv_l, axis=1)

        df = dlf / f - jnp.concatenate(dk_l, axis=1)
        dfl = df * (1.0 - lb) * sg * (1.0 - sg)
        dlb = jnp.sum(df * (1.0 - sg), axis=0, keepdims=True)
        dl0 = dlb * lb * (1.0 - lb)
        dlg_ref[...] += jnp.concatenate([dl0, -dl0], axis=0)[None]
        if second:
            sq = _sig(qraw)
            dqr = (dqp_ref[...] + dq_t) * (HGD ** -0.5) * (sq * (1.0 + qraw * (1.0 - sq)))
            dp_ref[...] = jnp.concatenate([dfl, dvp_ref[...] + dv_t, dqr], axis=1).astype(BF16)
        else:
            dp_ref[...] = dfl.astype(BF16)
            dv_ref[...] = dv_t
            dq_ref[...] = dq_t

    def col(cb):
        return pl.BlockSpec((TM, HGW), lambda s: (tile_of(s), cb))

    tok = pl.BlockSpec((TM, HGW), lambda s: (tile_of(s), 0))
    in_specs = [col(C_FB if rev else C_FF), col(C_INP), col(C_QHG),
                pl.BlockSpec((1, 2, HGW), lambda s: (d, 0, 0)),
                pl.BlockSpec((TM, HGW), lambda s: (jnp.maximum(tile_of(s) - 1, 0), 0)),
                pl.BlockSpec((ncht, 4, HGD, HGD), lambda s: (tile_of(s), 0, 0, 0))]
    args = [p, p, p, lg, do, st]
    dlg_spec = _full((1, 2, HGW))
    dlg_shape = jax.ShapeDtypeStruct((1, 2, HGW), F32)
    if second:
        in_specs += [tok, tok]
        args += [prev[0], prev[1]]
        out_specs = [pl.BlockSpec((TM, 3 * HGW), lambda s: (tile_of(s), 0)), dlg_spec]
        out_shape = [jax.ShapeDtypeStruct(dp.shape, BF16), dlg_shape]
    else:
        out_specs = [pl.BlockSpec((TM, HGW), lambda s: (tile_of(s), C_FB if rev else C_FF)), tok, tok, dlg_spec]
        out_shape = [jax.ShapeDtypeStruct(dp.shape, BF16), jax.ShapeDtypeStruct((tt, HGW), F32),
                     jax.ShapeDtypeStruct((tt, HGW), F32), dlg_shape]
    in_specs.append(ANY)
    args.append(dp)
    return _pcall(body, name="hgrn_bwd_rev" if rev else "hgrn_bwd", grid=(nt,),
                  in_specs=in_specs, out_specs=out_specs, out_shape=out_shape,
                  scratch=[pltpu.VMEM((4, HGD, HGD), F32)],
                  aliases={len(args) - 1: 0}, carry=carry)(*args)


def _head_rms(o, w, nheads):
    outs = []
    for h in range(nheads):
        oh = o[:, h * HGD:(h + 1) * HGD]
        outs.append(oh * lax.rsqrt(jnp.mean(oh * oh, axis=-1, keepdims=True) + EPS))
    return jnp.concatenate(outs, axis=1)


def _readout(o0, o1, p, hw4):
    s_len = o0.shape[0] - L

    def body(o0_ref, o1_ref, g_ref, w_ref, y_ref):
        xh = _head_rms(o0_ref[...] + o1_ref[...], None, 4)
        g = g_ref[...]
        y_ref[...] = (xh * w_ref[...] * (g * _sig(g))).astype(BF16)

    lat = pl.BlockSpec((TM, HGW), lambda i: (i + 1, 0))
    return _pcall(body, name="readout", grid=(s_len // TM,),
                  in_specs=[lat, lat, pl.BlockSpec((TM, HGW), lambda i: (i + 1, C_GHG)), _full((1, HGW))],
                  out_specs=pl.BlockSpec((TM, HGW), lambda i: (i, 0)),
                  out_shape=jax.ShapeDtypeStruct((s_len, HGW), BF16))(o0, o1, p, hw4)


def _readout_bwd(o0, o1, p, hw4, dy, dp, carry=None):
    tt = o0.shape[0]
    s_len = tt - L

    def body(o0_ref, o1_ref, g_ref, w_ref, dy_ref, _dp_in, dp_ref, do_ref, dw_ref):
        i = pl.program_id(0)

        @pl.when(i == 0)
        def _():
            dw_ref[...] = jnp.zeros_like(dw_ref)
            dp_ref[...] = jnp.zeros_like(dp_ref)

        @pl.when(i >= 1)
        def _():
            o = o0_ref[...] + o1_ref[...]
            g = g_ref[...]
            w = w_ref[...]
            sg = _sig(g)
            dy_ = dy_ref[...]
            dsw = dy_ * (g * sg)
            outs, xhs = [], []
            for h in range(4):
                sl = slice(h * HGD, (h + 1) * HGD)
                oh = o[:, sl]
                r = lax.rsqrt(jnp.mean(oh * oh, axis=-1, keepdims=True) + EPS)
                xh = oh * r
                dxh = dsw[:, sl] * w[:, sl]
                outs.append(r * (dxh - xh * jnp.mean(dxh * xh, axis=-1, keepdims=True)))
                xhs.append(xh)
            xh = jnp.concatenate(xhs, axis=1)
            do_ref[...] = jnp.concatenate(outs, axis=1)
            dp_ref[...] = (dy_ * xh * w * (sg * (1.0 + g * (1.0 - sg)))).astype(BF16)
            dw_ref[...] += jnp.sum(dsw * xh, axis=0, keepdims=True)

    tok = pl.BlockSpec((TM, HGW), lambda i: (i, 0))
    lat = pl.BlockSpec((TM, HGW), lambda i: (jnp.maximum(i - 1, 0), 0))
    return _pcall(body, name="readout_bwd", grid=(tt // TM,),
                  in_specs=[tok, tok, pl.BlockSpec((TM, HGW), lambda i: (i, C_GHG)), _full((1, HGW)), lat, ANY],
                  out_specs=[pl.BlockSpec((TM, HGW), lambda i: (i, C_GHG)), lat, _full((1, HGW))],
                  out_shape=[jax.ShapeDtypeStruct(dp.shape, BF16), jax.ShapeDtypeStruct((s_len, HGW), F32),
                             jax.ShapeDtypeStruct((1, HGW), F32)],
                  aliases={5: 0}, carry=carry)(o0, o1, p, hw4, dy, dp)


def _rope_tables(s_len):
    t = np.arange(s_len)
    inv = ROPE_THETA ** (-np.arange(0, 32, 2, dtype=np.float64) / 32)
    def half(pos):
        ang = pos[:, None].astype(np.float64) * inv[None, :]
        return (np.concatenate([np.cos(ang), np.cos(ang)], 1), np.concatenate([-np.sin(ang), np.sin(ang)], 1))
    cr, sr = half(t // GRID_W)
    cc, sc = half(t % GRID_W)
    cos = np.concatenate([cr, cc, cr, cc], 1)
    sin = np.concatenate([sr, sc, sr, sc], 1)
    cos = np.concatenate([np.ones((L, 128)), cos], 0)
    sin = np.concatenate([np.zeros((L, 128)), sin], 0)
    return jnp.asarray(cos, F32), jnp.asarray(sin, F32)


def _blockdiag(n, w):
    i = np.arange(n)
    return jnp.asarray((i[:, None] // w == i[None, :] // w) / float(w), F32)


def _dup_matrix():
    m = np.zeros((128, 512), np.float32)
    for g in range(2):
        for j in range(4):
            for dd in range(HDIM):
                m[64 * g + dd, 256 * g + 64 * j + dd] = 1.0
    return m


def _head_mean(x, blockdiag):
    return _dot(x, blockdiag, prec=lax.Precision.HIGH)


def _rot(x):
    n = x.shape[1]
    lane = lax.broadcasted_iota(jnp.int32, x.shape, 1)
    return jnp.where((lane % 32) < 16, pltpu.roll(x, n - 16, 1), pltpu.roll(x, 16, 1))


def _qk_prep(p, cos, sin, qnw8, knw2, bd512, bd128, dup):
    tt = p.shape[0]

    def body(q_ref, kv_ref, cos_ref, sin_ref, qw_ref, kw_ref, b5_ref, b1_ref, dup_ref,
             qr_ref, k4_ref, v4_ref):
        cos_, sin_ = cos_ref[...], sin_ref[...]
        q = q_ref[...]
        qn = q * lax.rsqrt(_head_mean(q * q, b5_ref[...]) + EPS) * qw_ref[...]
        cos4 = jnp.concatenate([cos_] * 4, axis=1)
        sin4 = jnp.concatenate([sin_] * 4, axis=1)
        qr_ref[...] = ((qn * cos4 + _rot(qn) * sin4) * (HDIM ** -0.5)).astype(BF16)
        kv = kv_ref[...]
        k, v = kv[:, :128], kv[:, 128:]
        kn = k * lax.rsqrt(_head_mean(k * k, b1_ref[...]) + EPS) * kw_ref[...]
        kr = kn * cos_ + _rot(kn) * sin_
        k4_ref[...] = _bdot(kr, dup_ref[...]).astype(BF16)
        v4_ref[...] = _bdot(v, dup_ref[...]).astype(BF16)

    row = lambda w, cb: pl.BlockSpec((TM, w), lambda i: (i, cb))
    out = jax.ShapeDtypeStruct((tt, ATW), BF16)
    return _pcall(body, name="qk_prep", grid=(tt // TM,),
                  in_specs=[row(ATW, C_QRAW), row(256, C_KV), row(128, 0), row(128, 0),
                            _full((1, ATW)), _full((1, 128)), _full((ATW, ATW)), _full((128, 128)),
                            _full((128, ATW))],
                  out_specs=[row(ATW, 0)] * 3, out_shape=[out] * 3)(
                      p, p, cos, sin, qnw8, knw2, bd512, bd128, dup)


def _attn_masks(i, nb):
    r = lax.broadcasted_iota(jnp.int32, (4 * BLK, 3 * BLK + L), 0) % BLK
    c = lax.broadcasted_iota(jnp.int32, (4 * BLK, 3 * BLK + L), 1)
    kpos = (i - 1) * BLK + c
    loc = (jnp.abs(c - BLK - r) <= BLK) & (kpos >= 0) & (kpos < nb * BLK)
    return loc | (c >= 3 * BLK)


def _stack_mask():
    r = lax.broadcasted_iota(jnp.int32, (4 * BLK, 256), 0)
    lane = lax.broadcasted_iota(jnp.int32, (4 * BLK, 256), 1)
    return (r // BLK) == (lane // HDIM)


def _stack_heads(xg, fill=0.0):
    x4 = jnp.concatenate([xg] * 4, axis=0)
    return jnp.where(_stack_mask(), x4, jnp.full_like(x4, fill))


def _unstack_heads(x4):
    out = jnp.where(_lane_mask(0), x4[0:BLK], 0.0)
    for j in range(1, 4):
        out = out + jnp.where(_lane_mask(j), x4[j * BLK:(j + 1) * BLK], 0.0)
    return out


def _per_head_rows(vals):
    return jnp.concatenate([jnp.broadcast_to(v, (BLK, 1)) for v in vals], axis=0)


def _lane_mask(j):
    lane = lax.broadcasted_iota(jnp.int32, (1, 256), 1)
    return (lane // HDIM) == j


def _attn_specs(nb):
    blk = lambda off: pl.BlockSpec((BLK, ATW), lambda i: (jnp.clip(i + off, 0, nb - 1) + 2, 0))
    ctx = pl.BlockSpec((L, ATW), lambda i: (0, 0))
    return blk, ctx


def _attn_fwd(qr, k4, v4, sinks, carry=None):
    tt = qr.shape[0]
    s_len = tt - L
    nb = s_len // BLK

    def body(sk_ref, q_ref, kp, ko, kn, kc, vp, vo, vn, vc, y_ref, lse_ref):
        i = pl.program_id(0)
        valid = _attn_masks(i, nb)
        q = q_ref[...]
        ys, lses = [], []
        for g in range(2):
            gs = slice(256 * g, 256 * g + 256)
            kcat = jnp.concatenate([kp[:, gs], ko[:, gs], kn[:, gs], kc[:, gs]], axis=0)
            vcat = jnp.concatenate([vp[:, gs], vo[:, gs], vn[:, gs], vc[:, gs]], axis=0)
            sink = _per_head_rows([sk_ref[4 * g + j] for j in range(4)])
            s = jnp.where(valid, _dot(_stack_heads(q[:, gs]), kcat, NT), -1e30)
            m = jnp.maximum(jnp.max(s, axis=-1, keepdims=True), sink)
            e = jnp.exp(s - m)
            den = jnp.sum(e, axis=-1, keepdims=True) + jnp.exp(sink - m)
            ys.append(_unstack_heads(_bdot(e * (1.0 / den), vcat)))
            lses.append(_unstack_heads(jnp.broadcast_to(m + jnp.log(den), (4 * BLK, 256))))
        y_ref[...] = jnp.concatenate(ys, axis=1).astype(BF16)
        lse_ref[...] = jnp.concatenate(lses, axis=1)

    blk, ctx = _attn_specs(nb)
    out = pl.BlockSpec((BLK, ATW), lambda i: (i, 0))
    return _pcall(body, name="attn_fwd", grid=(nb,),
                  in_specs=[pl.BlockSpec(memory_space=pltpu.SMEM), blk(0),
                            blk(-1), blk(0), blk(1), ctx, blk(-1), blk(0), blk(1), ctx],
                  out_specs=[out, out],
                  out_shape=[jax.ShapeDtypeStruct((s_len, ATW), BF16),
                             jax.ShapeDtypeStruct((s_len, ATW), F32)], carry=carry)(
                      sinks, qr, k4, k4, k4, k4, v4, v4, v4, v4)


def _attn_bwd(qr, k4, v4, sinks, y, lse, dy, carry=None):
    tt = qr.shape[0]
    s_len = tt - L
    nb = s_len // BLK

    def body(sk_ref, q_ref, kp, ko, kn, kc, vp, vo, vn, vc, y_ref, lse_ref, dy_ref,
             dq_ref, dkw_ref, dvw_ref, dkc_ref, dvc_ref, dsk_ref):
        i = pl.program_id(0)

        @pl.when(i == 0)
        def _():
            dkc_ref[...] = jnp.zeros_like(dkc_ref)
            dvc_ref[...] = jnp.zeros_like(dvc_ref)
            dsk_ref[...] = jnp.zeros_like(dsk_ref)

        valid = _attn_masks(i, nb)
        q = q_ref[...]
        dy_ = dy_ref[...]
        dly = dy_ * y_ref[...].astype(F32)
        lse_ = lse_ref[...]
        dqs = []
        for g in range(2):
            gs = slice(256 * g, 256 * g + 256)
            kcat = jnp.concatenate([kp[:, gs], ko[:, gs], kn[:, gs], kc[:, gs]], axis=0)
            vcat = jnp.concatenate([vp[:, gs], vo[:, gs], vn[:, gs], vc[:, gs]], axis=0)
            q4 = _stack_heads(q[:, gs])
            dy4 = _stack_heads(dy_[:, gs]).astype(BF16)
            lse4 = jnp.max(_stack_heads(lse_[:, gs], fill=-1e30), axis=-1, keepdims=True)
            delta = jnp.sum(_stack_heads(dly[:, gs]), axis=-1, keepdims=True)
            sink = _per_head_rows([sk_ref[4 * g + j] for j in range(4)])
            pr = jnp.where(valid, jnp.exp(_dot(q4, kcat, NT) - lse4), 0.0)
            dsb = (pr * (_dot(dy4, vcat, NT) - delta)).astype(BF16)
            dsink = jnp.exp(sink - lse4) * delta
            for j in range(4):
                dsk_ref[4 * g + j:4 * g + j + 1, :] += jnp.broadcast_to(
                    -jnp.sum(dsink[j * BLK:(j + 1) * BLK], axis=0, keepdims=True), (1, 128))
            dqs.append(_unstack_heads(_dot(dsb, kcat)))
            dkg = _dot(dsb, q4, TN)
            dvg = _dot(pr.astype(BF16), dy4, TN)
            dkw_ref[0, :, gs] = dkg[:3 * BLK]
            dvw_ref[0, :, gs] = dvg[:3 * BLK]
            dkc_ref[:, gs] += dkg[3 * BLK:]
            dvc_ref[:, gs] += dvg[3 * BLK:]
        dq_ref[...] = jnp.concatenate(dqs, axis=1)

    blk, ctx = _attn_specs(nb)
    out = pl.BlockSpec((BLK, ATW), lambda i: (i, 0))
    win = pl.BlockSpec((1, 3 * BLK, ATW), lambda i: (i, 0, 0))
    acc = _full((L, ATW))
    return _pcall(body, name="attn_bwd", grid=(nb,),
                  in_specs=[pl.BlockSpec(memory_space=pltpu.SMEM), blk(0),
                            blk(-1), blk(0), blk(1), ctx, blk(-1), blk(0), blk(1), ctx, out, out, out],
                  out_specs=[out, win, win, acc, acc, _full((8, 128))],
                  out_shape=[jax.ShapeDtypeStruct((s_len, ATW), F32),
                             jax.ShapeDtypeStruct((nb, 3 * BLK, ATW), F32),
                             jax.ShapeDtypeStruct((nb, 3 * BLK, ATW), F32),
                             jax.ShapeDtypeStruct((L, ATW), F32), jax.ShapeDtypeStruct((L, ATW), F32),
                             jax.ShapeDtypeStruct((8, 128), F32)], carry=carry)(
                      sinks, qr, k4, k4, k4, k4, v4, v4, v4, v4, y, lse, dy)


def _attn_post(p, cos, sin, qnw8, knw2, bd512, bd128, dupt, dq, dkw, dvw, dkc, dvc, dp, carry=None):
    tt = p.shape[0]
    s_len = tt - L
    nb = s_len // BLK
    nctx = L // BLK

    def body(q_ref, kv_ref, cos_ref, sin_ref, qw_ref, kw_ref, b5_ref, b1_ref, dupt_ref,
             dq_ref, kwp, kwo, kwn, vwp, vwo, vwn, dkc_ref, dvc_ref, _dp_in,
             dp_ref, dqw_ref, dkw_ref):
        t = pl.program_id(0)
        j = t - nctx

        @pl.when(t == 0)
        def _():
            dqw_ref[...] = jnp.zeros_like(dqw_ref)
            dkw_ref[...] = jnp.zeros_like(dkw_ref)

        is_lat = t >= nctx
        cos_, sin_ = cos_ref[...], sin_ref[...]
        has_p = is_lat & (j >= 1)
        has_n = is_lat & (j <= nb - 2)
        dk4 = (jnp.where(is_lat, kwo[0], dkc_ref[...]) + jnp.where(has_p, kwp[0], 0.0)
               + jnp.where(has_n, kwn[0], 0.0))
        dv4 = (jnp.where(is_lat, vwo[0], dvc_ref[...]) + jnp.where(has_p, vwp[0], 0.0)
               + jnp.where(has_n, vwn[0], 0.0))
        dkr = _dot(dk4, dupt_ref[...], prec=HI)
        dv = _dot(dv4, dupt_ref[...], prec=HI)
        kv = kv_ref[...]
        k = kv[:, :128]
        kw = kw_ref[...]
        rk = lax.rsqrt(_head_mean(k * k, b1_ref[...]) + EPS)
        xk = k * rk
        dkn = dkr * cos_ + _rot(dkr * sin_)
        dxk = dkn * kw
        dk = rk * (dxk - xk * _head_mean(dxk * xk, b1_ref[...]))
        dkw_ref[...] += jnp.sum(dkn * xk, axis=0, keepdims=True)
        q = q_ref[...]
        qw = qw_ref[...]
        rq = lax.rsqrt(_head_mean(q * q, b5_ref[...]) + EPS)
        xq = q * rq
        cos4 = jnp.concatenate([cos_] * 4, axis=1)
        sin4 = jnp.concatenate([sin_] * 4, axis=1)
        dqr = jnp.where(is_lat, dq_ref[...], 0.0) * (HDIM ** -0.5)
        dqn = dqr * cos4 + _rot(dqr * sin4)
        dxq = dqn * qw
        dqraw = rq * (dxq - xq * _head_mean(dxq * xq, b5_ref[...]))
        dqw_ref[...] += jnp.sum(dqn * xq, axis=0, keepdims=True)
        dp_ref[...] = jnp.concatenate([dqraw, dk, dv], axis=1).astype(BF16)

    row = lambda w, cb: pl.BlockSpec((BLK, w), lambda t: (t, cb))
    lat = pl.BlockSpec((BLK, ATW), lambda t: (jnp.maximum(t - nctx, 0), 0))

    def part(off):
        return pl.BlockSpec((1, BLK, ATW), lambda t: (jnp.clip(t - nctx + off, 0, nb - 1), 1 - off, 0))

    cacc = pl.BlockSpec((BLK, ATW), lambda t: (jnp.minimum(t, nctx - 1), 0))
    return _pcall(body, name="attn_post", grid=(tt // BLK,),
                  in_specs=[row(ATW, C_QRAW), row(256, C_KV), row(128, 0), row(128, 0),
                            _full((1, ATW)), _full((1, 128)), _full((ATW, ATW)), _full((128, 128)),
                            _full((ATW, 128)), lat, part(-1), part(0), part(1), part(-1), part(0), part(1),
                            cacc, cacc, ANY],
                  out_specs=[pl.BlockSpec((BLK, 768), lambda t: (t, C_QKV)), _full((1, ATW)), _full((1, 128))],
                  out_shape=[jax.ShapeDtypeStruct(dp.shape, BF16), jax.ShapeDtypeStruct((1, ATW), F32),
                             jax.ShapeDtypeStruct((1, 128), F32)],
                  aliases={18: 0}, carry=carry)(p, p, cos, sin, qnw8, knw2, bd512, bd128, dupt,
                                   dq, dkw, dkw, dkw, dvw, dvw, dvw, dkc, dvc, dp)


def _branch_merge(y_hg, y_at, bh4, ba4, p):
    s_len = y_hg.shape[0]

    def body(yh_ref, ya_ref, bh_ref, ba_ref, gh_ref, ga_ref, ah_ref, aa_ref, m_ref):
        yh, ya = yh_ref[...], ya_ref[...]
        ah = jnp.concatenate([_bdot(yh, bh_ref[j]) for j in range(4)], axis=1)
        aa = jnp.concatenate([_bdot(ya, ba_ref[j]) for j in range(4)], axis=1)
        ah_ref[...] = ah
        aa_ref[...] = aa
        m_ref[...] = (_sig(gh_ref[...]) * ah + _sig(ga_ref[...]) * aa).astype(BF16)

    row = pl.BlockSpec((TM, D), lambda i: (i, 0))
    y = pl.BlockSpec((TM, HGW), lambda i: (i, 0))
    f = jax.ShapeDtypeStruct((s_len, D), F32)
    return _pcall(body, name="branch_merge", grid=(s_len // TM,),
                  in_specs=[y, y, _full(bh4.shape), _full(ba4.shape),
                            pl.BlockSpec((TM, D), lambda i: (i + 1, 2)), pl.BlockSpec((TM, D), lambda i: (i + 1, 3))],
                  out_specs=[row, row, row],
                  out_shape=[f, f, jax.ShapeDtypeStruct((s_len, D), BF16)])(y_hg, y_at, bh4, ba4, p, p)


def _branch_bwd(dmh, dma, bh4, ba4, y_hg, y_at):
    s_len = dmh.shape[0]
    nk = s_len // TS
    ns = D // 4

    def body(dh_ref, da_ref, bh_ref, ba_ref, yh_ref, ya_ref, dyh_ref, dya_ref, gh_ref, ga_ref, acc_h, acc_a):
        t = pl.program_id(0)

        @pl.when(t == 0)
        def _():
            acc_h[...] = jnp.zeros_like(acc_h)
            acc_a[...] = jnp.zeros_like(acc_a)

        for d_ref, w_ref, y_ref, dy_ref, acc in ((dh_ref, bh_ref, yh_ref, dyh_ref, acc_h),
                                                 (da_ref, ba_ref, ya_ref, dya_ref, acc_a)):
            y = y_ref[...]
            dy = jnp.zeros((TS, HGW), F32)
            for j in range(4):
                dj = d_ref[:, j * ns:(j + 1) * ns]
                dy = dy + _bdot(dj, w_ref[j], NT)
                acc[j] += _bdot(y, dj, TN)
            dy_ref[...] = dy

        @pl.when(t == nk - 1)
        def _():
            gh_ref[...] = acc_h[...].astype(BF16)
            ga_ref[...] = acc_a[...].astype(BF16)

    dm = pl.BlockSpec((TS, D), lambda t: (t, 0))
    y = pl.BlockSpec((TS, HGW), lambda t: (t, 0))
    w = _full(bh4.shape)
    fy = jax.ShapeDtypeStruct((s_len, HGW), F32)
    gw = jax.ShapeDtypeStruct(bh4.shape, BF16)
    return _pcall(body, name="branch_bwd", grid=(nk,), in_specs=[dm, dm, w, w, y, y],
                  out_specs=[y, y, w, w], out_shape=[fy, fy, gw, gw],
                  scratch=[pltpu.VMEM(bh4.shape, F32)] * 2)(dmh, dma, bh4, ba4, y_hg, y_at)


def _merge_bwd(dattn, w_o, ah, aa, p, carry=None):
    tt = p.shape[0]
    s_len = tt - L

    def body(da_ref, wo_ref, ah_ref, aa_ref, gh_ref, ga_ref, dp_ref, dmh_ref, dma_ref):
        i = pl.program_id(0)

        @pl.when(i == 0)
        def _():
            dp_ref[...] = jnp.zeros_like(dp_ref)

        @pl.when(i >= 1)
        def _():
            dm_ = _bdot(da_ref[...], wo_ref[...], NT)
            sh, sa = _sig(gh_ref[...]), _sig(ga_ref[...])
            dp_ref[...] = jnp.concatenate([dm_ * ah_ref[...] * sh * (1.0 - sh),
                                           dm_ * aa_ref[...] * sa * (1.0 - sa)], axis=1).astype(BF16)
            dmh_ref[...] = (dm_ * sh).astype(BF16)
            dma_ref[...] = (dm_ * sa).astype(BF16)

    lat = pl.BlockSpec((TM, D), lambda i: (jnp.maximum(i - 1, 0), 0))
    return _pcall(body, name="merge_bwd", grid=(tt // TM,),
                  in_specs=[lat, _full((D, D)), lat, lat, pl.BlockSpec((TM, D), lambda i: (i, 2)),
                            pl.BlockSpec((TM, D), lambda i: (i, 3))],
                  out_specs=[pl.BlockSpec((TM, 2 * D), lambda i: (i, C_GATES)), lat, lat],
                  out_shape=[jax.ShapeDtypeStruct((tt, NCOL), BF16), jax.ShapeDtypeStruct((s_len, D), BF16),
                             jax.ShapeDtypeStruct((s_len, D), BF16)], carry=carry)(dattn, w_o, ah, aa, p, p)


def _local_step(x, ctx, tgt, mod, modc, nw1, nw2, lg, hw, qnw, knw, sinks,
                w_in, wts, dist=None):
    s_len = x.shape[0]
    tt = s_len + L
    ss1 = jnp.stack([modc, mod[0:2]])
    ss2 = mod[3:5][None]
    g1, g2 = mod[2:3], mod[5:6]
    hw4 = jnp.tile(hw, (1, 4))
    qnw8 = jnp.tile(qnw, (1, 8))
    knw2 = jnp.tile(knw, (1, 2))
    cos, sin = _rope_tables(s_len)
    bd512, bd128 = _blockdiag(ATW, HDIM), _blockdiag(128, HDIM)
    dupm = _dup_matrix()
    dup, dupt = jnp.asarray(dupm, BF16), jnp.asarray(dupm.T, F32)
    tmt = tt

    def four(b):
        return b.reshape(4, 2 * b.shape[1], b.shape[2])

    def halves(g):
        return g.reshape(4, 2, g.shape[1] // 2, g.shape[2])

    h = _mod1(ctx, x, nw1, ss1)
    if dist is None:
        bh4, ba4, w_o, g4, u4, dn4 = wts
        p = _mm_in(h, w_in, tmt)
        o0, st0 = _hgrn_fwd(p, lg, rev=False)
        o1, st1 = _hgrn_fwd(p, lg, rev=True)
    else:
        core, chip = dist
        half = wts[3].shape[1] // 2
        p, first = _mm_in(h, w_in, tmt, carry=_carry_join(_carry_gather(list(wts[0:3])),
                                                          _carry_gather([wts[3]], rows=[(0, half)])))
        (o0, st0), (g8,) = _hgrn_fwd(p, lg, rev=False, carry=_carry_gather([first[3]], rows=[(half, half)]))
        o1, st1 = _hgrn_fwd(p, lg, rev=True)
        bh4, ba4, w_o, g4 = four(first[0]), four(first[1]), four(first[2]).reshape(D, D), four(g8)
    y_hg = _readout(o0, o1, p, hw4)
    qr, k4, v4 = _qk_prep(p, cos, sin, qnw8, knw2, bd512, bd128, dup)
    if dist is None:
        y_at, lse = _attn_fwd(qr, k4, v4, sinks)
    else:
        (y_at, lse), (u8,) = _attn_fwd(qr, k4, v4, sinks, carry=_carry_gather([wts[4]]))
        u4 = four(u8)
    ah, aa, mixed = _branch_merge(y_hg, y_at, bh4, ba4, p)
    ao, x1, h2 = _out_proj_mod2(mixed, w_o, x, g1, nw2, ss2)
    if dist is None:
        a4, b4, z4 = _ffn_up(h2, g4, u4)
    else:
        (a4, b4, z4), (dn8,) = _ffn_up(h2, g4, u4, carry=_carry_gather([wts[5]]))
        dn4 = four(dn8)
    sq, dx2, dyb, dg2 = _ffn_down_loss(z4, dn4, x1, g2, tgt)

    da4, db4 = _ffn_dz(dyb, dn4, a4, b4)
    g_dn = _ffn_gdn(z4, dyb)
    if dist is None:
        dh2 = _ffn_dh2(da4, db4, g4, u4)
    else:
        dn_units = [halves(g_dn)]
        dh2, dn_recv = _ffn_dh2(da4, db4, g4, u4, carry=_carry_pairx(dn_units))
        dn_pairs = _rs_pair_add(dn_units, dn_recv, core)
    if dist is None:
        g_g, g_u = _ffn_ggu(h2, da4, db4)
    else:
        (g_g, g_u), c_dn = _ffn_ggu(h2, da4, db4, carry=_carry_chipx(dn_pairs))
        red_dn = _rs_chip_add(dn_pairs, c_dn, core, chip)
    dx1, dattn, dss2, dnw2, dg1 = _mod2_bwd(x1, dh2, dx2, ao, nw2, ss2, g1)
    g_o = _mm(mixed, dattn, name="mm_go", mode="tn", out_dtype=BF16, tm=D, tn=D, tk=512)
    if dist is None:
        dp, dmh, dma = _merge_bwd(dattn, w_o, ah, aa, p)
    else:
        gu_units = [halves(g_g), halves(g_u)]
        (dp, dmh, dma), gu_recv = _merge_bwd(dattn, w_o, ah, aa, p, carry=_carry_pairx(gu_units))
        ffn_pairs = list(dn_pairs) + list(_rs_pair_add(gu_units, gu_recv, core))
    dy_hg, dy_at, g_bh, g_ba = _branch_bwd(dmh, dma, bh4, ba4, y_hg, y_at)
    if dist is None:
        dp, do, dhw4 = _readout_bwd(o0, o1, p, hw4, dy_hg, dp)
        dq, dkw, dvw, dkc, dvc, dsk = _attn_bwd(qr, k4, v4, sinks, y_at, lse, dy_at)
        dp, dqnw8, dknw2 = _attn_post(p, cos, sin, qnw8, knw2, bd512, bd128, dupt, dq, dkw, dvw, dkc, dvc, dp)
    else:
        mix_units = [halves(g_bh), halves(g_ba), halves(g_o.reshape(4, D // 4, D))]
        (dp, do, dhw4), mix_recv = _readout_bwd(o0, o1, p, hw4, dy_hg, dp, carry=_carry_pairx(mix_units))
        mix_pairs = _rs_pair_add(mix_units, mix_recv, core)
        (dq, dkw, dvw, dkc, dvc, dsk), bwd = _attn_bwd(
            qr, k4, v4, sinks, y_at, lse, dy_at,
            carry=_carry_join(_carry_chipx(ffn_pairs[1:2]), _carry_sibx(red_dn)))
        red_g = _rs_chip_add(ffn_pairs[1:2], bwd[0:1], core, chip)
        (dp, dqnw8, dknw2), post = _attn_post(
            p, cos, sin, qnw8, knw2, bd512, bd128, dupt, dq, dkw, dvw, dkc, dvc, dp,
            carry=_carry_join(_carry_chipx(ffn_pairs[2:3]), _carry_sibx(red_g)))
        red_u = _rs_chip_add(ffn_pairs[2:3], post[0:1], core, chip)
    if dist is None:
        dp, dv0, dq0, dlg0 = _hgrn_bwd(p, lg, do, st0, dp, None, rev=False)
        dp, dlg1 = _hgrn_bwd(p, lg, do, st1, dp, (dv0, dq0), rev=True)
    else:
        (dp, dv0, dq0, dlg0), mid = _hgrn_bwd(p, lg, do, st0, dp, None, rev=False,
                                              carry=_carry_join(_carry_chipx(mix_pairs), _carry_sibx(red_u)))
        mix_reds = _rs_chip_add(mix_pairs, mid[0:3], core, chip)
        (dp, dlg1), mix_done = _hgrn_bwd(p, lg, do, st1, dp, (dv0, dq0), rev=True, carry=_carry_sibx(mix_reds))
        ffn_done = bwd[1:2] + post[1:2] + mid[3:4]
    g_in = _mm_gin(dp, h, tmt)
    if dist is None:
        dh = _mm_dh(dp, w_in, tmt)
        gx, dss1, dnw1 = _mod1_bwd(ctx, x, dh, dx1, nw1, ss1)
        rs = None
    else:
        in_units = [halves(g_in.reshape(4, NCOL // 4, D))]
        dh, in_recv = _mm_dh(dp, w_in, tmt, carry=_carry_pairx(in_units))
        in_pairs = _rs_pair_add(in_units, in_recv, core)
        first_rows = (0, in_pairs[0].shape[1] // 2)
        (gx, dss1, dnw1), in_part = _mod1_bwd(ctx, x, dh, dx1, nw1, ss1,
                                              carry=_carry_chipx(in_pairs, rows=first_rows))
        rs = dict(ffn_done=ffn_done, mix_done=mix_done, in_pairs=in_pairs, in_part=in_part)

    dmod = jnp.concatenate([dss1[1], dg1, dss2, dg2], axis=0)
    dmodc = dss1[0]
    raw = (dss1, dg1, dss2, dg2, dnw1, dnw2, dhw4, dqnw8, dknw2, dsk, dlg0, dlg1)
    small = dict(raw=raw, dmod=dmod, dmodc=dmodc, dnw1=dnw1, dnw2=dnw2,
                 dhw=dhw4.reshape(4, HGD).sum(0, keepdims=True),
                 dqnw=dqnw8.reshape(8, HDIM).sum(0, keepdims=True),
                 dknw=dknw2.reshape(2, HDIM).sum(0, keepdims=True),
                 dsinks=dsk[:, 0], dlg=jnp.concatenate([dlg0, dlg1], axis=0))
    big = dict(w_in=g_in, w_bh=g_bh, w_ba=g_ba, w_o=g_o, w_g=g_g, w_u=g_u, w_dn=g_dn)
    return sq, gx, big, small, rs


def _place():
    x, y, c = lax.axis_index("x"), lax.axis_index("y"), lax.axis_index("c")
    return x, y, c


def _gather_blocks(x_refs, out_refs, send_sems, recv_sems, local_sems):
    n = len(out_refs)
    x, y, c = _place()
    me, sibling = (x, y, c), (x, y, 1 - c)
    chips = [(1 - x, y), (x, 1 - y), (1 - x, 1 - y)]

    def slot(u, px, py, pc):
        return out_refs[u].at[4 * px + 2 * py + pc]

    def copy(u, k, block, to, src=None):
        return pltpu.make_async_remote_copy(
            src_ref=slot(u, *block) if src is None else src, dst_ref=slot(u, *block),
            send_sem=send_sems.at[u, k], recv_sem=recv_sems.at[u, k], device_id=to, device_id_type=MESH)

    mines = []
    if x_refs is not None:
        mines = [pltpu.make_async_copy(x_refs[u], slot(u, *me), local_sems.at[u]) for u in range(n)]
    for cp in mines:
        cp.start()
    first = []
    for u in range(n):
        src = None if x_refs is None else x_refs[u]
        first.append(copy(u, 0, me, sibling, src=src))
        first += [copy(u, 1 + j, me, (*chip, c), src=src) for j, chip in enumerate(chips)]
    for cp in first:
        cp.start()
    passed = []
    for j, chip in enumerate(chips):
        for u in range(n):
            copy(u, 1 + j, (*chip, c), me).wait_recv()
            fwd = copy(u, 4 + j, (*chip, c), sibling)
            fwd.start()
            passed.append(fwd)
    for u in range(n):
        copy(u, 0, sibling, me).wait_recv()
    for j, chip in enumerate(chips):
        for u in range(n):
            copy(u, 4 + j, (*chip, 1 - c), me).wait_recv()
    for cp in first + passed:
        cp.wait_send()
    for cp in mines:
        cp.wait()


def _gather_sems(n):
    return [pltpu.SemaphoreType.DMA((n, 7)), pltpu.SemaphoreType.DMA((n, 7)), pltpu.SemaphoreType.DMA((n,))]


def _allgather(blks, *, name, in_vmem):
    n = len(blks)
    space = pltpu.VMEM if in_vmem else pl.ANY

    def body(*refs):
        _gather_blocks(refs[:n], refs[n:2 * n], *refs[2 * n:])

    return pl.pallas_call(
        body, name=name, out_shape=[jax.ShapeDtypeStruct((8,) + b.shape, b.dtype) for b in blks],
        in_specs=[pl.BlockSpec(memory_space=space)] * n, out_specs=[pl.BlockSpec(memory_space=space)] * n,
        scratch_shapes=_gather_sems(n))(*blks)


def _cast_place(ws, c, dev):
    n = len(ws)

    def body(s_ref, *refs):
        for u in range(n):
            refs[n + u][0] = refs[u][...].astype(BF16)

    in_specs, out_specs, out_shape = [], [], []
    for w in ws:
        q, cols = w.shape[0] // 4, w.shape[1]
        in_specs.append(pl.BlockSpec((q, cols), lambda i, s: (2 * s[0] + i, 0)))
        out_specs.append(pl.BlockSpec((1, q, cols), lambda i, s: (s[1], i, 0)))
        out_shape.append(jax.ShapeDtypeStruct((8, 2 * q, cols), BF16))
    return pl.pallas_call(
        body, name="cast_place",
        grid_spec=pltpu.PrefetchScalarGridSpec(num_scalar_prefetch=1, grid=(2,), in_specs=in_specs,
                                               out_specs=out_specs),
        out_shape=_out_hbm(out_shape),
        compiler_params=pltpu.CompilerParams(vmem_limit_bytes=48 << 20))(jnp.stack([c, dev]), *_in_hbm(ws))


def _gather_phases(out_refs, send_sems, recv_sems, rows=None):
    n = len(out_refs)
    x, y, c = _place()
    me, sibling = (x, y, c), (x, y, 1 - c)
    chips = [(1 - x, y), (x, 1 - y), (1 - x, 1 - y)]

    def copy(u, k, block, to):
        px, py, pc = block
        ref = out_refs[u].at[4 * px + 2 * py + pc]
        if rows is not None and rows[u] is not None:
            ref = ref.at[pl.ds(rows[u][0], rows[u][1])]
        return pltpu.make_async_remote_copy(src_ref=ref, dst_ref=ref, send_sem=send_sems.at[u, k],
                                            recv_sem=recv_sems.at[u, k], device_id=to, device_id_type=MESH)

    def start():
        for u in range(n):
            copy(u, 0, me, sibling).start()
            for j, chip in enumerate(chips):
                copy(u, 1 + j, me, (*chip, c)).start()

    def mid():
        for j, chip in enumerate(chips):
            for u in range(n):
                copy(u, 1 + j, (*chip, c), me).wait_recv()
                copy(u, 4 + j, (*chip, c), sibling).start()

    def end():
        for u in range(n):
            copy(u, 0, sibling, me).wait_recv()
        for j, chip in enumerate(chips):
            for u in range(n):
                copy(u, 4 + j, (*chip, 1 - c), me).wait_recv()
        for u in range(n):
            copy(u, 0, me, sibling).wait_send()
            for j, chip in enumerate(chips):
                copy(u, 1 + j, me, (*chip, c)).wait_send()
                copy(u, 4 + j, (*chip, c), sibling).wait_send()

    return start, mid, end


def _carry_gather(bufs, rows=None):
    n = len(bufs)
    return _Carry(bufs, [jax.ShapeDtypeStruct(b.shape, b.dtype) for b in bufs], {u: u for u in range(n)},
                  [pltpu.SemaphoreType.DMA((n, 7)), pltpu.SemaphoreType.DMA((n, 7))],
                  lambda ins, outs, sems: _gather_phases(outs, *sems, rows=rows))


def _allgather_inplace(bufs, *, name):
    n = len(bufs)

    def body(*refs):
        for phase in _gather_phases(refs[n:2 * n], *refs[2 * n:]):
            phase()

    return pl.pallas_call(
        body, name=name, out_shape=[jax.ShapeDtypeStruct(b.shape, b.dtype) for b in bufs],
        in_specs=[ANY] * n, out_specs=[ANY] * n, input_output_aliases={u: u for u in range(n)},
        scratch_shapes=[pltpu.SemaphoreType.DMA((n, 7)), pltpu.SemaphoreType.DMA((n, 7))])(*bufs)


def _ag_small(raw):
    def body(dss1, dg1, dss2, dg2, dnw1, dnw2, dhw4, dqnw8, dknw2, dsk, dlg0, dlg1,
             out_ref, tot_ref, blk, send_sems, recv_sems, local_sems):
        blk[...] = jnp.zeros_like(blk)
        blk[0:2, :] = dss1[1]
        blk[2:3, :] = dg1[...]
        blk[3:5, :] = dss2[...]
        blk[5:6, :] = dg2[...]
        blk[6:8, :] = dss1[0]
        blk[8:9, :] = dnw1[...]
        blk[9:10, :] = dnw2[...]
        blk[10:11, 0:HGW] = dhw4[...]
        blk[10:11, HGW:D] = dqnw8[...]
        blk[11:12, 0:128] = dknw2[...]
        blk[12:14, 0:HGW] = dlg0[0]
        blk[14:16, 0:HGW] = dlg1[0]
        blk[16:24, 0:128] = dsk[...]
        _gather_blocks([blk], [out_ref], send_sems, recv_sems, local_sems)
        acc = out_ref[0]
        for i in range(1, 8):
            acc = acc + out_ref[i]
        tot_ref[...] = acc

    vm = pl.BlockSpec(memory_space=pltpu.VMEM)
    return pl.pallas_call(
        body, name="ag_small",
        out_shape=[jax.ShapeDtypeStruct((8, 24, D), F32), jax.ShapeDtypeStruct((24, D), F32)],
        in_specs=[vm] * 12, out_specs=[vm, vm],
        scratch_shapes=[pltpu.VMEM((24, D), F32)] + _gather_sems(1))(*raw)


def _rs_pair_exchange(units):
    n = len(units)

    def body(*refs):
        start, _, end = _pairx_phases(refs[:n], refs[n:2 * n], *refs[2 * n:])
        start()
        end()

    return pl.pallas_call(
        body, name="rs_pair_exchange", out_shape=_pairx_shapes(units),
        in_specs=[ANY] * n, out_specs=[ANY] * n,
        scratch_shapes=[pltpu.SemaphoreType.DMA((n, 4)), pltpu.SemaphoreType.DMA((n, 4))])(*units)


def _pairx_shapes(units):
    return [jax.ShapeDtypeStruct((4,) + g.shape[2:], g.dtype) for g in units]


def _pairx_phases(g_refs, r_refs, send_sems, recv_sems):
    n = len(g_refs)
    x, y, c = _place()
    cps = [pltpu.make_async_remote_copy(
        src_ref=g_refs[u].at[j, 1 - c], dst_ref=r_refs[u].at[j], send_sem=send_sems.at[u, j],
        recv_sem=recv_sems.at[u, j], device_id=(x, y, 1 - c), device_id_type=MESH)
        for u in range(n) for j in range(4)]

    def start():
        for cp in cps:
            cp.start()

    def end():
        for cp in cps:
            cp.wait()

    return start, None, end


def _carry_pairx(units):
    n = len(units)
    return _Carry(units, _pairx_shapes(units), {},
                  [pltpu.SemaphoreType.DMA((n, 4)), pltpu.SemaphoreType.DMA((n, 4))],
                  lambda ins, outs, sems: _pairx_phases(ins, outs, *sems))


def _rs_pair_add(units, recvs, c):
    n = len(units)

    def body(c_ref, *refs):
        for u in range(n):
            refs[2 * n + u][...] = (refs[u][0].astype(F32) + refs[n + u][...].astype(F32)).astype(BF16)

    in_specs, out_specs, out_shape = [], [], []
    for g in units:
        h, w = g.shape[2] // 2, g.shape[3]
        in_specs.append(pl.BlockSpec((1, 1, h, w), lambda j, i, cr: (j, cr[0], i, 0)))
    for g in units:
        h, w = g.shape[2] // 2, g.shape[3]
        in_specs.append(pl.BlockSpec((1, h, w), lambda j, i, cr: (j, i, 0)))
        out_specs.append(pl.BlockSpec((1, h, w), lambda j, i, cr: (j, i, 0)))
        out_shape.append(jax.ShapeDtypeStruct((4, 2 * h, w), BF16))
    return pl.pallas_call(
        body, name="rs_pair_add",
        grid_spec=pltpu.PrefetchScalarGridSpec(num_scalar_prefetch=1, grid=(4, 2), in_specs=in_specs,
                                               out_specs=out_specs),
        out_shape=_out_hbm(out_shape),
        compiler_params=pltpu.CompilerParams(vmem_limit_bytes=48 << 20))(
            c.reshape(1), *_in_hbm(list(units) + list(recvs)))


def _rs_chip_exchange(pairs):
    n = len(pairs)

    def body(*refs):
        start, _, end = _chipx_phases(refs[:n], refs[n:2 * n], *refs[2 * n:])
        start()
        end()

    return pl.pallas_call(
        body, name="rs_chip_exchange", out_shape=[jax.ShapeDtypeStruct(p.shape, p.dtype) for p in pairs],
        in_specs=[ANY] * n, out_specs=[ANY] * n,
        scratch_shapes=[pltpu.SemaphoreType.DMA((n, 3)), pltpu.SemaphoreType.DMA((n, 3))])(*pairs)


def _chipx_phases(p_refs, r_refs, send_sems, recv_sems, rows=None):
    n = len(p_refs)
    x, y, c = _place()
    k = 2 * x + y

    def part(ref):
        return ref if rows is None else ref.at[pl.ds(rows[0], rows[1])]

    sends = []
    for d in range(1, 4):
        j = (k + d) % 4
        for u in range(n):
            sends.append(pltpu.make_async_remote_copy(
                src_ref=part(p_refs[u].at[j]), dst_ref=part(r_refs[u].at[k]), send_sem=send_sems.at[u, d - 1],
                recv_sem=recv_sems.at[u, d - 1], device_id=(j // 2, j % 2, c), device_id_type=MESH))

    def start():
        for cp in sends:
            cp.start()

    def end():
        for d in range(1, 4):
            src = (k + 4 - d) % 4
            for u in range(n):
                pltpu.make_async_remote_copy(
                    src_ref=part(p_refs[u].at[src]), dst_ref=part(r_refs[u].at[src]),
                    send_sem=send_sems.at[u, d - 1], recv_sem=recv_sems.at[u, d - 1], device_id=(x, y, c),
                    device_id_type=MESH).wait_recv()
        for cp in sends:
            cp.wait_send()

    return start, None, end


def _carry_chipx(pairs, rows=None, into=None):
    n = len(pairs)
    sems = [pltpu.SemaphoreType.DMA((n, 3)), pltpu.SemaphoreType.DMA((n, 3))]
    shapes = [jax.ShapeDtypeStruct(p.shape, p.dtype) for p in pairs]
    if into is None:
        return _Carry(pairs, shapes, {}, sems, lambda ins, outs, s: _chipx_phases(ins, outs, *s, rows=rows))
    return _Carry(list(pairs) + list(into), shapes, {n + u: u for u in range(n)}, sems,
                  lambda ins, outs, s: _chipx_phases(ins[:n], outs, *s, rows=rows))


def _rs_chip_add(pairs, contribs, c, chip):
    n = len(pairs)

    def body(s_ref, *refs):
        for u in range(n):
            a, b, c_, d = refs[4 * u:4 * u + 4]
            refs[4 * n + u][0] = ((a[0].astype(F32) + b[0].astype(F32)) + c_[0].astype(F32)) + d[0].astype(F32)

    in_specs, out_specs, out_shape, args = [], [], [], []
    for p, r in zip(pairs, contribs):
        h, w = p.shape[1] // 2, p.shape[2]
        in_specs += [pl.BlockSpec((1, h, w), functools.partial(lambda d, i, s: ((s[1] + d) % 4, i, 0), d))
                     for d in range(4)]
        args += [p, r, r, r]
        out_specs.append(pl.BlockSpec((1, h, w), lambda i, s: (s[0], i, 0)))
        out_shape.append(jax.ShapeDtypeStruct((2, 2 * h, w), F32))
    return pl.pallas_call(
        body, name="rs_chip_add",
        grid_spec=pltpu.PrefetchScalarGridSpec(num_scalar_prefetch=1, grid=(2,), in_specs=in_specs,
                                               out_specs=out_specs),
        out_shape=_out_hbm(out_shape),
        compiler_params=pltpu.CompilerParams(vmem_limit_bytes=48 << 20))(jnp.stack([c, chip]), *_in_hbm(args))


def _rs_sibling_gather(reds):
    n = len(reds)

    def body(*refs):
        start, _, end = _sibx_phases(refs[n:2 * n], *refs[2 * n:])
        start()
        end()

    return pl.pallas_call(
        body, name="rs_sibling_gather", out_shape=[jax.ShapeDtypeStruct(r.shape, r.dtype) for r in reds],
        in_specs=[ANY] * n, out_specs=[ANY] * n, input_output_aliases={u: u for u in range(n)},
        scratch_shapes=[pltpu.SemaphoreType.DMA((n,))] * 2)(*reds)


def _sibx_phases(o_refs, send_sems, recv_sems):
    n = len(o_refs)
    x, y, c = _place()
    cps = [pltpu.make_async_remote_copy(
        src_ref=o_refs[u].at[c], dst_ref=o_refs[u].at[c], send_sem=send_sems.at[u], recv_sem=recv_sems.at[u],
        device_id=(x, y, 1 - c), device_id_type=MESH) for u in range(n)]

    def start():
        for cp in cps:
            cp.start()

    def end():
        for u in range(n):
            cps[u].wait_send()
            pltpu.make_async_remote_copy(
                src_ref=o_refs[u].at[1 - c], dst_ref=o_refs[u].at[1 - c], send_sem=send_sems.at[u],
                recv_sem=recv_sems.at[u], device_id=(x, y, 1 - c), device_id_type=MESH).wait_recv()

    return start, None, end


def _carry_sibx(reds):
    n = len(reds)
    return _Carry(reds, [jax.ShapeDtypeStruct(r.shape, r.dtype) for r in reds], {u: u for u in range(n)},
                  [pltpu.SemaphoreType.DMA((n,))] * 2, lambda ins, outs, sems: _sibx_phases(outs, *sems))


def _prologue(blk, c_ctx, w, b, in8):
    n = w.shape[1]

    def body(blk_ref, cctx_ref, w_ref, b_ref, _in_in, g0_ref, c16_ref, g1_ref, in_ref, mod_s,
             s1, r1, l1, s2, r2, l2, s3, r3):
        start, mid, end = _gather_phases([in_ref], s3, r3)
        start()
        _gather_blocks([blk_ref], [g0_ref], s1, r1, l1)
        c16 = jnp.concatenate([g0_ref[i, 0:1, :] for i in range(8)] + [cctx_ref[...], jnp.zeros((7, D), F32)],
                              axis=0)
        c16_ref[...] = c16
        mod_s[...] = _dot(c16 * _sig(c16), w_ref[...], prec=HI) + b_ref[...]
        _gather_blocks([mod_s], [g1_ref], s2, r2, l2)
        mid()
        end()

    vm = pl.BlockSpec(memory_space=pltpu.VMEM)
    return pl.pallas_call(
        body, name="prologue",
        out_shape=[jax.ShapeDtypeStruct((8, 8, D), F32), jax.ShapeDtypeStruct((16, D), F32),
                   jax.ShapeDtypeStruct((8, 16, n), F32), jax.ShapeDtypeStruct(in8.shape, in8.dtype)],
        in_specs=[vm, vm, vm, vm, ANY], out_specs=[vm, vm, vm, ANY], input_output_aliases={4: 3},
        scratch_shapes=[pltpu.VMEM((16, n), F32)] + _gather_sems(1) + _gather_sems(1)
        + [pltpu.SemaphoreType.DMA((1, 7)), pltpu.SemaphoreType.DMA((1, 7))],
        compiler_params=pltpu.CompilerParams(vmem_limit_bytes=48 << 20))(blk, c_ctx, w, b, in8)


def _ada_bwd(c16, dmod16, w):
    n = w.shape[1]
    tn = 512

    def body(c_ref, d_ref, w_ref, gw_ref, gc_ref):
        j = pl.program_id(0)

        @pl.when(j == 0)
        def _():
            gc_ref[...] = jnp.zeros_like(gc_ref)

        cc = c_ref[...]
        dm = d_ref[...]
        gw_ref[...] = _dot(cc * _sig(cc), dm, TN, prec=HI)
        gc_ref[...] += _dot(dm, w_ref[...], NT, prec=HI)

    return _pcall(body, name="ada_bwd", grid=(n // tn,),
                  in_specs=[_full((16, D)), pl.BlockSpec((16, tn), lambda j: (0, j)),
                            pl.BlockSpec((D, tn), lambda j: (0, j))],
                  out_specs=[pl.BlockSpec((D, tn), lambda j: (0, j)), _full((16, D))],
                  out_shape=[jax.ShapeDtypeStruct((D, n), F32),
                             jax.ShapeDtypeStruct((16, D), F32)])(c16, dmod16, w)


def _adam_math(w, g, m, v):
    c1 = 1.0 - ADAM_B1 ** ADAM_STEP
    c2 = 1.0 - ADAM_B2 ** ADAM_STEP
    nm = ADAM_B1 * m + (1.0 - ADAM_B1) * g
    nv = ADAM_B2 * v + (1.0 - ADAM_B2) * (g * g)
    return -ADAM_LR * ((nm / c1) / (jnp.sqrt(nv / c2) + ADAM_EPS) + ADAM_WD * w), nm, nv


def _adamw_small(ws, gs, ms, vs):
    n = len(ws)

    def body(*refs):
        for u in range(n):
            d_, nm, nv = _adam_math(refs[u][...], refs[n + u][...], refs[2 * n + u][...], refs[3 * n + u][...])
            refs[4 * n + u][...] = d_
            refs[5 * n + u][...] = nm
            refs[6 * n + u][...] = nv

    specs = [_full(w.shape) for w in ws]
    shapes = [jax.ShapeDtypeStruct(w.shape, F32) for w in ws]
    out = _pcall(body, name="adamw_small", grid=(1,), in_specs=specs * 4, out_specs=specs * 3,
                 out_shape=shapes * 3)(*ws, *gs, *ms, *vs)
    return out[:n], out[n:2 * n], out[2 * n:]


def _cctx_grad(parts, c_ctx):
    def body(p_ref, c_ref, o_ref):
        acc = p_ref[0:1, :]
        for k in range(1, 4):
            acc = acc + p_ref[k:k + 1, :]
        cc = c_ref[...]
        s = _sig(cc)
        o_ref[...] = acc * (s * (1.0 + cc * (1.0 - s)))

    return _pcall(body, name="cctx_grad", grid=(1,), in_specs=[_full(parts.shape), _full((1, D))],
                  out_specs=_full((1, D)), out_shape=jax.ShapeDtypeStruct((1, D), F32))(parts, c_ctx)


ADAM_STEPS = 8


def _adamw_multi(ws, gs, ms, vs, *, name, carry=None):
    n = len(ws)

    def body(*refs):
        for u in range(n):
            refs[4 * n + u][...], refs[5 * n + u][...], refs[6 * n + u][...] = _adam_math(
                refs[u][...], refs[n + u][...], refs[2 * n + u][...], refs[3 * n + u][...])

    specs = [pl.BlockSpec((w.shape[0] // ADAM_STEPS, w.shape[1]), lambda i: (i, 0)) for w in ws]
    shapes = [jax.ShapeDtypeStruct(w.shape, F32) for w in ws]
    res = _pcall(body, name=name, grid=(ADAM_STEPS,), in_specs=specs * 4, out_specs=specs * 3,
                 out_shape=shapes * 3, carry=carry)(*ws, *gs, *ms, *vs)
    out, extra = res if carry is not None else (res, None)
    return (out[:n], out[n:2 * n], out[2 * n:]), extra


def kernel(x, c, ctx, c_ctx, w_ada, b_ada, norm_mix_w, norm_ffn_w, w_in, hgrn_lb_logits, hgrn_norm_w, q_norm_w, k_norm_w, attn_sinks, w_branch_hgrn, w_branch_attn, w_out, w_ffn_gate, w_ffn_up, w_ffn_down, loss_target, m_c_ctx, m_w_ada, m_b_ada, m_norm_mix_w, m_norm_ffn_w, m_w_in, m_hgrn_lb_logits, m_hgrn_norm_w, m_q_norm_w, m_k_norm_w, m_attn_sinks, m_w_branch_hgrn, m_w_branch_attn, m_w_out, m_w_ffn_gate, m_w_ffn_up, m_w_ffn_down, v_c_ctx, v_w_ada, v_b_ada, v_norm_mix_w, v_norm_ffn_w, v_w_in, v_hgrn_lb_logits, v_hgrn_norm_w, v_q_norm_w, v_k_norm_w, v_attn_sinks, v_w_branch_hgrn, v_w_branch_attn, v_w_out, v_w_ffn_gate, v_w_ffn_up, v_w_ffn_down):
    xi, yi, ci = _place()
    chip = 2 * xi + yi
    dev = 2 * chip + ci
    s_len = x.shape[1]

    shards = [w_in[0].T, w_branch_hgrn[0], w_branch_attn[0], w_out[0], w_ffn_gate[0].T, w_ffn_up[0].T,
              w_ffn_down[0]]
    bufs = _cast_place(shards, ci, dev)

    lbrow = jnp.pad(hgrn_lb_logits.reshape(1, 512), ((0, 0), (0, D - 512)))
    blk = jnp.concatenate([c, lbrow, jnp.zeros((6, D), F32)], axis=0)
    nada = w_ada.shape[2]
    b_sh = lax.dynamic_slice(b_ada, (0, chip * nada), (1, nada))
    g0, c16, g1, in8 = _prologue(blk, c_ctx[None], w_ada[0], b_sh, bufs[0])
    lg = g0[0::2, 1, :512].reshape(4, 2, 2, 128).transpose(1, 2, 0, 3).reshape(2, 2, HGW)
    modall = g1[0::2].transpose(1, 0, 2).reshape(16, 4 * nada)
    mod = lax.dynamic_slice(modall, (dev, 0), (1, 6 * D)).reshape(6, D)
    modc = modall[8].reshape(6, D)[:2]

    sq, gx, _, small, rs = _local_step(
        x[0], ctx[0], loss_target[0], mod, modc, norm_mix_w, norm_ffn_w, lg, hgrn_norm_w, q_norm_w,
        k_norm_w, attn_sinks[0], in8.reshape(NCOL, D), bufs[1:], dist=(ci, chip))
    loss = lax.psum(0.5 * jnp.sum(sq) / D, ("x", "y", "c"))

    def whole(r):
        return r.reshape(2 * r.shape[1], r.shape[2])

    g_dn, g_g, g_u = [whole(r) for r in rs["ffn_done"]]
    g_bh, g_ba, g_o = [whole(r) for r in rs["mix_done"]]
    in_pairs = rs["in_pairs"]
    rest_rows = (in_pairs[0].shape[1] // 2, in_pairs[0].shape[1] // 2)

    g2, tot = _ag_small(small["raw"])
    dmodc_tot = jnp.pad(tot[6:8].reshape(1, 2 * D), ((0, 0), (0, 4 * D)))
    g_b_ada = tot[0:6].reshape(1, 6 * D) + dmodc_tot
    dmod16 = jnp.concatenate([g2[:, 0:6].reshape(8, 6 * D), dmodc_tot, jnp.zeros((7, 6 * D), F32)], axis=0)
    g_w_ada, gc_part = _ada_bwd(c16, lax.dynamic_slice(dmod16, (0, chip * nada), (16, nada)), w_ada[0])
    g3, = _allgather([gc_part[8:16]], name="ag_cctx", in_vmem=True)
    g_c_ctx = _cctx_grad(g3[0::2, 0], c_ctx[None])[0]
    g_nw1 = tot[8:9]
    g_nw2 = tot[9:10]
    g_hw = tot[10, :HGW].reshape(4, HGD).sum(0, keepdims=True)
    g_qnw = tot[10, HGW:].reshape(8, HDIM).sum(0, keepdims=True)
    g_knw = tot[11, :128].reshape(2, HDIM).sum(0, keepdims=True)
    g_sinks = tot[16:24, 0][None]
    g_lg = lax.dynamic_slice(tot[12:16, :HGW].reshape(2, 2, HGW), (0, 0, chip * 128), (2, 2, 128))

    names = ["c_ctx", "w_ada", "b_ada", "norm_mix_w", "norm_ffn_w", "w_in", "hgrn_lb_logits", "hgrn_norm_w",
             "q_norm_w", "k_norm_w", "attn_sinks", "w_branch_hgrn", "w_branch_attn", "w_out", "w_ffn_gate",
             "w_ffn_up", "w_ffn_down"]
    ws = dict(zip(names, [c_ctx, w_ada, b_ada, norm_mix_w, norm_ffn_w, w_in, hgrn_lb_logits, hgrn_norm_w,
                          q_norm_w, k_norm_w, attn_sinks, w_branch_hgrn, w_branch_attn, w_out, w_ffn_gate,
                          w_ffn_up, w_ffn_down]))
    ms = dict(zip(names, [m_c_ctx, m_w_ada, m_b_ada, m_norm_mix_w, m_norm_ffn_w, m_w_in, m_hgrn_lb_logits,
                          m_hgrn_norm_w, m_q_norm_w, m_k_norm_w, m_attn_sinks, m_w_branch_hgrn,
                          m_w_branch_attn, m_w_out, m_w_ffn_gate, m_w_ffn_up, m_w_ffn_down]))
    vs = dict(zip(names, [v_c_ctx, v_w_ada, v_b_ada, v_norm_mix_w, v_norm_ffn_w, v_w_in, v_hgrn_lb_logits,
                          v_hgrn_norm_w, v_q_norm_w, v_k_norm_w, v_attn_sinks, v_w_branch_hgrn,
                          v_w_branch_attn, v_w_out, v_w_ffn_gate, v_w_ffn_up, v_w_ffn_down]))
    transposed = ("w_in", "w_ffn_gate", "w_ffn_up")

    def view(a, n):
        return a[0].T if n in transposed else a[0]

    def unview(a, n):
        return a.T[None] if n in transposed else a[None]

    delta, new_m, new_v, grads = {}, {}, {}, {}

    def big_adamw(group, gs, name, carry=None):
        (d_, m_, v_), extra = _adamw_multi([view(ws[n], n) for n in group], gs, [view(ms[n], n) for n in group],
                                           [view(vs[n], n) for n in group], name=name, carry=carry)
        for i, n in enumerate(group):
            grads[n], delta[n], new_m[n], new_v[n] = (unview(gs[i], n), unview(d_[i], n), unview(m_[i], n),
                                                      unview(v_[i], n))
        return extra

    in_contribs = big_adamw(["w_ffn_down", "w_ffn_gate", "w_ffn_up", "w_out", "w_branch_hgrn", "w_branch_attn"],
                            [g_dn, g_g, g_u, g_o, g_bh, g_ba], "adamw_first",
                            carry=_carry_chipx(in_pairs, rows=rest_rows, into=rs["in_part"]))
    in_reds = _rs_chip_add(in_pairs, in_contribs, ci, chip)
    g_in, = [whole(r) for r in _rs_sibling_gather(in_reds)]
    big_adamw(["w_in", "w_ada"], [g_in, g_w_ada], "adamw_second")
    grads.update(c_ctx=g_c_ctx, b_ada=g_b_ada, norm_mix_w=g_nw1, norm_ffn_w=g_nw2, hgrn_lb_logits=g_lg,
                 hgrn_norm_w=g_hw, q_norm_w=g_qnw, k_norm_w=g_knw, attn_sinks=g_sinks)
    small_names = [n for n in names if n not in delta]

    def two_d(a):
        return a.reshape(1, -1) if a.ndim == 1 else a

    sd, sm_, sv = _adamw_small(*[[two_d(d[n]) for n in small_names] for d in (ws, grads, ms, vs)])
    for i, n in enumerate(small_names):
        for dst, src in ((delta, sd), (new_m, sm_), (new_v, sv)):
            dst[n] = src[i].reshape(ws[n].shape)
    return (loss, gx[None], *[grads[n] for n in names], *[delta[n] for n in names],
            *[new_m[n] for n in names], *[new_v[n] for n in names])
```

```python
import functools

import numpy as np
import jax
import jax.numpy as jnp
from jax import lax
from jax.experimental import pallas as pl
from jax.experimental.pallas import tpu as pltpu

F32 = jnp.float32
BF16 = jnp.bfloat16
HI = lax.Precision.HIGHEST
MESH = pl.DeviceIdType.MESH

D = 1024
L = 256
TM = 256
HGW = 512
HGD = 128
CH = 32
ATW = 512
HDIM = 64
BLK = 128
GRID_W = 64
DFF = 2816
NCOL = 5376
EPS = 1e-6
ROPE_THETA = 10000.0

C_FB, C_INP, C_QHG, C_FF = 0, 1, 2, 3
C_GATES = 1
C_GHG, C_QRAW = 8, 9
C_KV = 20
C_QKV = 6

ADAM_LR, ADAM_B1, ADAM_B2, ADAM_EPS, ADAM_WD, ADAM_STEP = 0.001, 0.9, 0.999, 1e-08, 0.01, 10

NN = (((1,), (0,)), ((), ()))
NT = (((1,), (1,)), ((), ()))
TN = (((0,), (0,)), ((), ()))


def _dot(a, b, dims=NN, prec=None):
    return lax.dot_general(a, b, dims, precision=prec, preferred_element_type=F32)


def _bdot(a, b, dims=NN):
    return _dot(a.astype(BF16), b.astype(BF16), dims)


def _sig(x):
    return 1.0 / (1.0 + jnp.exp(-x))


class _Carry:
    def __init__(self, ins, outs, aliases, scratch, phases):
        self.ins, self.outs, self.aliases, self.scratch, self.phases = ins, outs, aliases, scratch, phases


def _in_hbm(args):
    return [pltpu.with_memory_space_constraint(a, pltpu.HBM) for a in args]


def _out_hbm(shapes):
    if isinstance(shapes, (list, tuple)):
        return [pltpu.HBM(s.shape, s.dtype) for s in shapes]
    return pltpu.HBM(shapes.shape, shapes.dtype)


def _carry_join(a, b):
    na_in, na_out, na_sc = len(a.ins), len(a.outs), len(a.scratch)
    aliases = dict(a.aliases)
    aliases.update({na_in + i: na_out + o for i, o in b.aliases.items()})

    def phases(ins, outs, sems):
        pa = a.phases(ins[:na_in], outs[:na_out], sems[:na_sc])
        pb = b.phases(ins[na_in:], outs[na_out:], sems[na_sc:])

        def both(fa, fb):
            if fa is None and fb is None:
                return None

            def run():
                for fn in (fa, fb):
                    if fn is not None:
                        fn()
            return run

        return tuple(both(fa, fb) for fa, fb in zip(pa, pb))

    return _Carry(list(a.ins) + list(b.ins), list(a.outs) + list(b.outs), aliases,
                  list(a.scratch) + list(b.scratch), phases)


def _pcall(body, *, name, grid, in_specs, out_specs, out_shape, scratch=(), aliases=None, vmem_mb=48,
           carry=None):
    params = pltpu.CompilerParams(dimension_semantics=("arbitrary",) * len(grid),
                                  vmem_limit_bytes=vmem_mb << 20)
    if carry is None:
        plain = pl.pallas_call(
            body, name=name, grid=grid, in_specs=in_specs, out_specs=out_specs, out_shape=_out_hbm(out_shape),
            scratch_shapes=list(scratch), input_output_aliases=aliases or {}, compiler_params=params)
        return lambda *args: plain(*_in_hbm(args))
    single = not isinstance(out_shape, (list, tuple))
    out_specs_l = [out_specs] if single else list(out_specs)
    out_shape_l = [out_shape] if single else list(out_shape)
    n_in, n_out, n_sc = len(in_specs), len(out_shape_l), len(scratch)
    k_in, k_out = len(carry.ins), len(carry.outs)
    nsteps = int(np.prod(grid))
    assert nsteps >= 3

    def wrapped(*refs):
        ins, cins = refs[:n_in], refs[n_in:n_in + k_in]
        o0 = n_in + k_in
        outs, couts = refs[o0:o0 + n_out], refs[o0 + n_out:o0 + n_out + k_out]
        s0 = o0 + n_out + k_out
        sc, csc = refs[s0:s0 + n_sc], refs[s0 + n_sc:]
        step = pl.program_id(0)
        for ax in range(1, len(grid)):
            step = step * grid[ax] + pl.program_id(ax)
        start, mid, end = carry.phases(cins, couts, csc)
        pl.when(step == 0)(start)
        body(*ins, *outs, *sc)
        if mid is not None:
            pl.when(step == nsteps - 2)(mid)
        pl.when(step == nsteps - 1)(end)

    all_aliases = dict(aliases or {})
    all_aliases.update({n_in + i: n_out + o for i, o in carry.aliases.items()})
    call = pl.pallas_call(
        wrapped, name=name, grid=grid, in_specs=list(in_specs) + [ANY] * k_in,
        out_specs=out_specs_l + [ANY] * k_out, out_shape=_out_hbm(out_shape_l + list(carry.outs)),
        scratch_shapes=list(scratch) + list(carry.scratch), input_output_aliases=all_aliases,
        compiler_params=params)

    def run(*args):
        res = call(*_in_hbm(args), *carry.ins)
        core = res[:n_out]
        return (core[0] if single else list(core)), list(res[n_out:])

    return run


def _full(shape):
    nd = len(shape)
    return pl.BlockSpec(shape, lambda *_: (0,) * nd)


ANY = pl.BlockSpec(memory_space=pl.ANY)


def _mm(a, b, *, name, mode="nn", out_dtype=F32, tm, tn, tk):
    if mode == "nn":
        (m, k), (k2, n) = a.shape, b.shape
    elif mode == "nt":
        (m, k), (n, k2) = a.shape, b.shape
    else:
        (k, m), (k2, n) = a.shape, b.shape
    assert k == k2 and m % tm == 0 and n % tn == 0 and k % tk == 0, (name, a.shape, b.shape)
    nk = k // tk
    dims = {"nn": NN, "nt": NT, "tn": TN}[mode]

    def body(a_ref, b_ref, o_ref, acc):
        kk = pl.program_id(2)

        @pl.when(kk == 0)
        def _():
            acc[...] = jnp.zeros_like(acc)

        acc[...] += _bdot(a_ref[...], b_ref[...], dims)

        @pl.when(kk == nk - 1)
        def _():
            o_ref[...] = acc[...].astype(out_dtype)

    a_spec = (pl.BlockSpec((tk, tm), lambda i, j, kk: (kk, i)) if mode == "tn"
              else pl.BlockSpec((tm, tk), lambda i, j, kk: (i, kk)))
    b_spec = (pl.BlockSpec((tn, tk), lambda i, j, kk: (j, kk)) if mode == "nt"
              else pl.BlockSpec((tk, tn), lambda i, j, kk: (kk, j)))
    return _pcall(body, name=name, grid=(m // tm, n // tn, nk), in_specs=[a_spec, b_spec],
                  out_specs=pl.BlockSpec((tm, tn), lambda i, j, kk: (i, j)),
                  out_shape=jax.ShapeDtypeStruct((m, n), out_dtype),
                  scratch=[pltpu.VMEM((tm, tn), F32)])(a, b)


NT_IN = NCOL // 256


def _src_block(j):
    return j + jnp.where(j < 4, 2, jnp.where(j < 6, 3, jnp.where(j < 8, -6, jnp.where(
        j < 16, 5, jnp.where(j < 20, -7, -14)))))


def _mm_in(h, wt, tm, carry=None):
    tt = h.shape[0]

    def body(h_ref, w_ref, o_ref):
        o_ref[...] = _bdot(h_ref[...], w_ref[...], NT)

    return _pcall(body, name="mm_in", grid=(tt // tm, NT_IN),
                  in_specs=[pl.BlockSpec((tm, D), lambda i, j: (i, 0)),
                            pl.BlockSpec((256, D), lambda i, j: (_src_block(j), 0))],
                  out_specs=pl.BlockSpec((tm, 256), lambda i, j: (i, j)),
                  out_shape=jax.ShapeDtypeStruct((tt, NCOL), F32), carry=carry)(h, wt)


def _mm_dh(dp, wt, tm, carry=None):
    tt = dp.shape[0]
    per, ng = 3, NT_IN // 3

    def body(d_ref, w0, w1, w2, o_ref, acc):
        kk = pl.program_id(1)

        @pl.when(kk == 0)
        def _():
            acc[...] = jnp.zeros_like(acc)

        acc[...] += (_bdot(d_ref[:, 0:256], w0[...]) + _bdot(d_ref[:, 256:512], w1[...])
                     + _bdot(d_ref[:, 512:768], w2[...]))

        @pl.when(kk == ng - 1)
        def _():
            o_ref[...] = acc[...]

    wspecs = [pl.BlockSpec((256, D), functools.partial(lambda t, i, kk: (_src_block(per * kk + t), 0), t))
              for t in range(per)]
    return _pcall(body, name="mm_dh", grid=(tt // tm, ng),
                  in_specs=[pl.BlockSpec((tm, per * 256), lambda i, kk: (i, kk))] + wspecs,
                  out_specs=pl.BlockSpec((tm, D), lambda i, kk: (i, 0)),
                  out_shape=jax.ShapeDtypeStruct((tt, D), F32), scratch=[pltpu.VMEM((tm, D), F32)],
                  carry=carry)(dp, wt, wt, wt)


def _mm_gin(dp, h, tk):
    tt = dp.shape[0]
    nk = tt // tk

    def body(d_ref, h_ref, o_ref, acc):
        kk = pl.program_id(1)

        @pl.when(kk == 0)
        def _():
            acc[...] = jnp.zeros_like(acc)

        acc[...] += _bdot(d_ref[...], h_ref[...], TN)

        @pl.when(kk == nk - 1)
        def _():
            o_ref[...] = acc[...].astype(BF16)

    return _pcall(body, name="mm_gin", grid=(NT_IN, nk),
                  in_specs=[pl.BlockSpec((tk, 256), lambda j, kk: (kk, j)),
                            pl.BlockSpec((tk, D), lambda j, kk: (kk, 0))],
                  out_specs=pl.BlockSpec((256, D), lambda j, kk: (_src_block(j), 0)),
                  out_shape=jax.ShapeDtypeStruct((NCOL, D), BF16), scratch=[pltpu.VMEM((256, D), F32)])(dp, h)


def _tok_specs():
    assert L == TM
    return [_full((TM, D)), pl.BlockSpec((TM, D), lambda i: (jnp.maximum(i - 1, 0), 0))]


def _mod1(ctx, x, nw, ss):
    rows = L + x.shape[0]

    def body(c_ref, x_ref, nw_ref, ss_ref, h_ref):
        t = jnp.where(pl.program_id(0) == 0, c_ref[...], x_ref[...])
        r = lax.rsqrt(jnp.mean(t * t, axis=-1, keepdims=True) + EPS)
        s = ss_ref[0]
        h_ref[...] = ((t * r * nw_ref[...]) * (1.0 + s[1:2]) + s[0:1]).astype(BF16)

    return _pcall(body, name="mod1", grid=(rows // TM,),
                  in_specs=_tok_specs() + [_full((1, D)),
                                           pl.BlockSpec((1, 2, D), lambda i: (jnp.minimum(i, 1), 0, 0))],
                  out_specs=pl.BlockSpec((TM, D), lambda i: (i, 0)),
                  out_shape=jax.ShapeDtypeStruct((rows, D), BF16))(ctx, x, nw, ss)


def _norm_bwd_rows(x, dh, nw, scale):
    r = lax.rsqrt(jnp.mean(x * x, axis=-1, keepdims=True) + EPS)
    xh = x * r
    dxh = dh * ((1.0 + scale) * nw)
    dx = r * (dxh - xh * jnp.mean(dxh * xh, axis=-1, keepdims=True))
    return dx, xh


def _out_proj_mod2(mixed, w_o, x, g1, nw2, ss2):
    s_len = x.shape[0]
    tm = 512

    def body(m_ref, w_ref, x_ref, g_ref, nw_ref, ss_ref, ao_ref, x1_ref, h_ref):
        ao = _bdot(m_ref[...], w_ref[...])
        ao_ref[...] = ao
        x1 = x_ref[...] + g_ref[...] * ao
        x1_ref[...] = x1
        r = lax.rsqrt(jnp.mean(x1 * x1, axis=-1, keepdims=True) + EPS)
        s = ss_ref[0]
        h_ref[...] = ((x1 * r * nw_ref[...]) * (1.0 + s[1:2]) + s[0:1]).astype(BF16)

    row = pl.BlockSpec((tm, D), lambda i: (i, 0))
    f = jax.ShapeDtypeStruct((s_len, D), F32)
    return _pcall(body, name="out_proj_mod2", grid=(s_len // tm,),
                  in_specs=[row, _full((D, D)), row, _full((1, D)), _full((1, D)), _full((1, 2, D))],
                  out_specs=[row, row, row],
                  out_shape=[f, f, jax.ShapeDtypeStruct((s_len, D), BF16)])(mixed, w_o, x, g1, nw2, ss2)


TS = 1024


def _acc_call(body, *, name, grid, in_specs, out_specs, out_shape, acc_shapes, args, carry=None):
    return _pcall(body, name=name, grid=grid, in_specs=in_specs, out_specs=out_specs, out_shape=out_shape,
                  scratch=[pltpu.VMEM(s, F32) for s in acc_shapes], carry=carry)(*args)


def _mm_cs(a, w4, *, name):
    m, k = a.shape
    _, _, ns = w4.shape

    def body(a_ref, w_ref, o_ref):
        o_ref[...] = _bdot(a_ref[...], w_ref[0])

    return _pcall(body, name=name, grid=(m // TS, 4),
                  in_specs=[pl.BlockSpec((TS, k), lambda i, j: (i, 0)),
                            pl.BlockSpec((1, k, ns), lambda i, j: (j, 0, 0))],
                  out_specs=pl.BlockSpec((TS, ns), lambda i, j: (i, j)),
                  out_shape=jax.ShapeDtypeStruct((m, 4 * ns), F32))(a, w4)


def _mm_cs_nt(a, w4, *, name):
    m = a.shape[0]
    _, k, ns = w4.shape

    def body(a_ref, w_ref, o_ref, acc):
        j = pl.program_id(1)

        @pl.when(j == 0)
        def _():
            acc[...] = jnp.zeros_like(acc)

        acc[...] += _bdot(a_ref[...], w_ref[0], NT)

        @pl.when(j == 3)
        def _():
            o_ref[...] = acc[...]

    return _acc_call(body, name=name, grid=(m // TS, 4),
                     in_specs=[pl.BlockSpec((TS, ns), lambda i, j: (i, j)),
                               pl.BlockSpec((1, k, ns), lambda i, j: (j, 0, 0))],
                     out_specs=pl.BlockSpec((TS, k), lambda i, j: (i, 0)),
                     out_shape=jax.ShapeDtypeStruct((m, k), F32), acc_shapes=[(TS, k)], args=(a, w4))


def _mm_cs_tn(a, b, ns, *, name):
    s_len, k = a.shape
    nk = s_len // TS

    def body(a_ref, b_ref, o_ref, acc):
        t = pl.program_id(1)

        @pl.when(t == 0)
        def _():
            acc[...] = jnp.zeros_like(acc)

        acc[...] += _bdot(a_ref[...], b_ref[...], TN)

        @pl.when(t == nk - 1)
        def _():
            o_ref[0] = acc[...].astype(o_ref.dtype)

    return _acc_call(body, name=name, grid=(4, nk),
                     in_specs=[pl.BlockSpec((TS, k), lambda j, t: (t, 0)),
                               pl.BlockSpec((TS, ns), lambda j, t: (t, j))],
                     out_specs=pl.BlockSpec((1, k, ns), lambda j, t: (j, 0, 0)),
                     out_shape=jax.ShapeDtypeStruct((4, k, ns), BF16), acc_shapes=[(k, ns)], args=(a, b))


def _ffn_up(h2, g4, u4, carry=None):
    s_len = h2.shape[0]
    ns = g4.shape[1]

    def body(h_ref, g_ref, u_ref, a_ref, b_ref, z_ref):
        h = h_ref[...]
        a = _bdot(h, g_ref[0], NT)
        b = _bdot(h, u_ref[0], NT)
        a_ref[0] = a.astype(BF16)
        b_ref[0] = b.astype(BF16)
        z_ref[0] = (a * _sig(a) * b).astype(BF16)

    w = pl.BlockSpec((1, ns, D), lambda i, j: (j, 0, 0))
    o = pl.BlockSpec((1, TS, ns), lambda i, j: (j, i, 0))
    f = jax.ShapeDtypeStruct((4, s_len, ns), BF16)
    return _pcall(body, name="ffn_up", grid=(s_len // TS, 4),
                  in_specs=[pl.BlockSpec((TS, D), lambda i, j: (i, 0)), w, w], out_specs=[o, o, o],
                  out_shape=[f, f, jax.ShapeDtypeStruct((4, s_len, ns), BF16)], carry=carry)(h2, g4, u4)


def _ffn_down_loss(z4, dn4, x1, g2, tgt):
    _, s_len, ns = z4.shape

    def body(z_ref, w_ref, x1_ref, g_ref, t_ref, sq_ref, dx2_ref, dyb_ref, dg_ref, acc):
        i, j = pl.program_id(0), pl.program_id(1)

        @pl.when((i == 0) & (j == 0))
        def _():
            sq_ref[...] = jnp.zeros_like(sq_ref)
            dg_ref[...] = jnp.zeros_like(dg_ref)

        @pl.when(j == 0)
        def _():
            acc[...] = jnp.zeros_like(acc)

        acc[...] += _bdot(z_ref[0], w_ref[0])

        @pl.when(j == 3)
        def _():
            y_ = acc[...]
            g = g_ref[...]
            e = x1_ref[...] + g * y_ - t_ref[...]
            sq_ref[...] += jnp.sum(e * e, axis=0, keepdims=True)
            dx2 = e * (1.0 / D)
            dx2_ref[...] = dx2
            dyb_ref[...] = (g * dx2).astype(BF16)
            dg_ref[...] += jnp.sum(dx2 * y_, axis=0, keepdims=True)

    row = pl.BlockSpec((TS, D), lambda i, j: (i, 0))
    vec = _full((1, D))
    return _acc_call(body, name="ffn_down_loss", grid=(s_len // TS, 4),
                     in_specs=[pl.BlockSpec((1, TS, ns), lambda i, j: (j, i, 0)),
                               pl.BlockSpec((1, ns, D), lambda i, j: (j, 0, 0)), row, vec, row],
                     out_specs=[vec, row, row, vec],
                     out_shape=[jax.ShapeDtypeStruct((1, D), F32), jax.ShapeDtypeStruct((s_len, D), F32),
                                jax.ShapeDtypeStruct((s_len, D), BF16), jax.ShapeDtypeStruct((1, D), F32)],
                     acc_shapes=[(TS, D)], args=(z4, dn4, x1, g2, tgt))


def _ffn_dz(dyb, dn4, a4, b4):
    _, s_len, ns = a4.shape

    def body(dy_ref, w_ref, a_ref, b_ref, da_ref, db_ref):
        dz = _bdot(dy_ref[...], w_ref[0], NT)
        a = a_ref[0].astype(F32)
        s = _sig(a)
        da_ref[0] = (dz * b_ref[0].astype(F32) * (s * (1.0 + a * (1.0 - s)))).astype(BF16)
        db_ref[0] = (dz * (a * s)).astype(BF16)

    t = pl.BlockSpec((1, TS, ns), lambda i, j: (j, i, 0))
    o = jax.ShapeDtypeStruct((4, s_len, ns), BF16)
    return _pcall(body, name="ffn_dz", grid=(s_len // TS, 4),
                  in_specs=[pl.BlockSpec((TS, D), lambda i, j: (i, 0)),
                            pl.BlockSpec((1, ns, D), lambda i, j: (j, 0, 0)), t, t],
                  out_specs=[t, t], out_shape=[o, o])(dyb, dn4, a4, b4)


def _ffn_gdn(z4, dyb):
    _, s_len, ns = z4.shape
    nk = s_len // TS

    def body(z_ref, dy_ref, o_ref, acc):
        t = pl.program_id(1)

        @pl.when(t == 0)
        def _():
            acc[...] = jnp.zeros_like(acc)

        acc[...] += _bdot(z_ref[0], dy_ref[...], TN)

        @pl.when(t == nk - 1)
        def _():
            o_ref[0] = acc[...].astype(o_ref.dtype)

    return _acc_call(body, name="ffn_gdn", grid=(4, nk),
                     in_specs=[pl.BlockSpec((1, TS, ns), lambda j, t: (j, t, 0)),
                               pl.BlockSpec((TS, D), lambda j, t: (t, 0))],
                     out_specs=pl.BlockSpec((1, ns, D), lambda j, t: (j, 0, 0)),
                     out_shape=jax.ShapeDtypeStruct((4, ns, D), BF16), acc_shapes=[(ns, D)], args=(z4, dyb))


def _ffn_dh2(da4, db4, g4, u4, carry=None):
    _, s_len, ns = da4.shape

    def body(da_ref, db_ref, g_ref, u_ref, o_ref, acc):
        j = pl.program_id(1)

        @pl.when(j == 0)
        def _():
            acc[...] = jnp.zeros_like(acc)

        acc[...] += _bdot(da_ref[0], g_ref[0]) + _bdot(db_ref[0], u_ref[0])

        @pl.when(j == 3)
        def _():
            o_ref[...] = acc[...]

    t = pl.BlockSpec((1, TS, ns), lambda i, j: (j, i, 0))
    w = pl.BlockSpec((1, ns, D), lambda i, j: (j, 0, 0))
    return _acc_call(body, name="ffn_dh2", grid=(s_len // TS, 4), in_specs=[t, t, w, w],
                     out_specs=pl.BlockSpec((TS, D), lambda i, j: (i, 0)),
                     out_shape=jax.ShapeDtypeStruct((s_len, D), F32), acc_shapes=[(TS, D)],
                     args=(da4, db4, g4, u4), carry=carry)


def _ffn_ggu(h2, da4, db4, carry=None):
    _, s_len, ns = da4.shape
    nk = s_len // TS

    def body(h_ref, da_ref, db_ref, gg_ref, gu_ref, acc_g, acc_u):
        t = pl.program_id(1)

        @pl.when(t == 0)
        def _():
            acc_g[...] = jnp.zeros_like(acc_g)
            acc_u[...] = jnp.zeros_like(acc_u)

        h = h_ref[...]
        acc_g[...] += _bdot(da_ref[0], h, TN)
        acc_u[...] += _bdot(db_ref[0], h, TN)

        @pl.when(t == nk - 1)
        def _():
            gg_ref[0] = acc_g[...].astype(BF16)
            gu_ref[0] = acc_u[...].astype(BF16)

    d = pl.BlockSpec((1, TS, ns), lambda j, t: (j, t, 0))
    o = pl.BlockSpec((1, ns, D), lambda j, t: (j, 0, 0))
    f = jax.ShapeDtypeStruct((4, ns, D), BF16)
    return _acc_call(body, name="ffn_ggu", grid=(4, nk),
                     in_specs=[pl.BlockSpec((TS, D), lambda j, t: (t, 0)), d, d], out_specs=[o, o],
                     out_shape=[f, f], acc_shapes=[(ns, D), (ns, D)], args=(h2, da4, db4), carry=carry)


def _mod2_bwd(x1, dh2, dx2, ao, nw2, ss2, g1):
    s_len = x1.shape[0]

    def body(x1_ref, dh_ref, dx2_ref, ao_ref, nw_ref, ss_ref, g_ref,
             dx1_ref, da_ref, dss_ref, dnw_ref, dg_ref):
        i = pl.program_id(0)

        @pl.when(i == 0)
        def _():
            dss_ref[...] = jnp.zeros_like(dss_ref)
            dnw_ref[...] = jnp.zeros_like(dnw_ref)
            dg_ref[...] = jnp.zeros_like(dg_ref)

        dh = dh_ref[...]
        nw = nw_ref[...]
        scale = ss_ref[0][1:2]
        dxn, xh = _norm_bwd_rows(x1_ref[...], dh, nw, scale)
        dx1 = dx2_ref[...] + dxn
        dx1_ref[...] = dx1
        da_ref[...] = (g_ref[...] * dx1).astype(BF16)
        dg_ref[...] += jnp.sum(dx1 * ao_ref[...], axis=0, keepdims=True)
        dsh = jnp.sum(dh, axis=0, keepdims=True)
        dsc = jnp.sum(dh * xh * nw, axis=0, keepdims=True)
        dss_ref[...] += jnp.concatenate([dsh, dsc], axis=0)
        dnw_ref[...] += jnp.sum(dh * xh * (1.0 + scale), axis=0, keepdims=True)

    row = pl.BlockSpec((TM, D), lambda i: (i, 0))
    vec = _full((1, D))
    return _pcall(body, name="mod2_bwd", grid=(s_len // TM,),
                  in_specs=[row, row, row, row, vec, _full((1, 2, D)), vec],
                  out_specs=[row, row, _full((2, D)), vec, vec],
                  out_shape=[jax.ShapeDtypeStruct((s_len, D), F32), jax.ShapeDtypeStruct((s_len, D), BF16),
                             jax.ShapeDtypeStruct((2, D), F32), jax.ShapeDtypeStruct((1, D), F32),
                             jax.ShapeDtypeStruct((1, D), F32)])(x1, dh2, dx2, ao, nw2, ss2, g1)


def _mod1_bwd(ctx, x, dh, dx1, nw1, ss1, carry=None):
    s_len = dx1.shape[0]
    tt = L + s_len

    def body(c_ref, x_ref, dh_ref, dx1_ref, nw_ref, ss_ref, dx_ref, dss_ref, dnw_ref):
        i = pl.program_id(0)
        tok = jnp.where(i == 0, c_ref[...], x_ref[...])

        @pl.when(i == 0)
        def _():
            dnw_ref[...] = jnp.zeros_like(dnw_ref)

        @pl.when(i <= 1)
        def _():
            dss_ref[...] = jnp.zeros_like(dss_ref)

        dh_ = dh_ref[...]
        nw = nw_ref[...]
        scale = ss_ref[0][1:2]
        dxn, xh = _norm_bwd_rows(tok, dh_, nw, scale)

        @pl.when(i >= 1)
        def _():
            dx_ref[...] = dx1_ref[...] + dxn

        dsh = jnp.sum(dh_, axis=0, keepdims=True)
        dsc = jnp.sum(dh_ * xh * nw, axis=0, keepdims=True)
        dss_ref[...] += jnp.concatenate([dsh, dsc], axis=0)[None]
        dnw_ref[...] += jnp.sum(dh_ * xh * (1.0 + scale), axis=0, keepdims=True)

    row = pl.BlockSpec((TM, D), lambda i: (i, 0))
    lat = pl.BlockSpec((TM, D), lambda i: (jnp.maximum(i - 1, 0), 0))
    sel = pl.BlockSpec((1, 2, D), lambda i: (jnp.minimum(i, 1), 0, 0))
    return _pcall(body, name="mod1_bwd", grid=(tt // TM,),
                  in_specs=_tok_specs() + [row, lat, _full((1, D)), sel],
                  out_specs=[lat, sel, _full((1, D))],
                  out_shape=[jax.ShapeDtypeStruct((s_len, D), F32), jax.ShapeDtypeStruct((2, 2, D), F32),
                             jax.ShapeDtypeStruct((1, D), F32)], carry=carry)(ctx, x, dh, dx1, nw1, ss1)


def _rows(c):
    return slice(c * CH, (c + 1) * CH)


def _chunk_masks(rev, transpose=False):
    r = lax.broadcasted_iota(jnp.int32, (TM, TM), 0)
    c = lax.broadcasted_iota(jnp.int32, (TM, TM), 1)
    same = (r // CH) == (c // CH)
    before = (c >= r) if (rev != transpose) else (c <= r)
    return same & before, same


def _chunk_scan(x, rev, transpose=False):
    r = lax.broadcasted_iota(jnp.int32, (CH, CH), 0)
    c = lax.broadcasted_iota(jnp.int32, (CH, CH), 1)
    tri = ((c >= r) if (rev != transpose) else (c <= r)).astype(F32)
    return jnp.concatenate([_dot(tri, x[_rows(ch)], prec=HI) for ch in range(x.shape[0] // CH)], axis=0)


def _chunk_total(x):
    return jnp.concatenate([jnp.broadcast_to(jnp.sum(x[_rows(ch)], axis=0, keepdims=True), (CH, x.shape[1]))
                            for ch in range(x.shape[0] // CH)], axis=0)


def _hgrn_gate(fl, qraw, lg):
    lb = 1.0 / (1.0 + jnp.exp(lg[1:2] - lg[0:1]))
    sg = _sig(fl)
    f = lb + (1.0 - lb) * sg
    q = qraw * _sig(qraw) * (HGD ** -0.5)
    return lb, sg, f, q


def _hgrn_fwd(p, lg, *, rev, carry=None):
    tt = p.shape[0]
    nt = tt // TM
    ncht = TM // CH
    d = 1 if rev else 0

    def tile_of(s):
        return jnp.where(s == 0, 0, nt - s) if rev else s

    def body(f_ref, inp_ref, q_ref, lg_ref, o_ref, st_ref, state):
        s = pl.program_id(0)

        @pl.when(s == 0)
        def _():
            state[...] = jnp.zeros_like(state)

        _, _, f, q = _hgrn_gate(f_ref[...], q_ref[...], lg_ref[0])
        lf = jnp.log(f)
        causal, _ = _chunk_masks(rev)
        cum = _chunk_scan(lf, rev)
        tot = _chunk_total(lf)
        qd = (q * jnp.exp(cum)).astype(BF16)
        kd = ((1.0 - f) * jnp.exp(-cum)).astype(BF16)
        ke = ((1.0 - f) * jnp.exp(tot - cum)).astype(BF16)
        et = jnp.exp(tot)
        v = inp_ref[...].astype(BF16)
        order = range(ncht - 1, -1, -1) if rev else range(ncht)
        outs = []
        for h in range(4):
            sl = slice(h * HGD, (h + 1) * HGD)
            qd_, kd_, ke_, v_ = qd[:, sl], kd[:, sl], ke[:, sl], v[:, sl]
            pm = jnp.where(causal, _dot(qd_, kd_, NT), 0.0).astype(BF16)
            o_h = _dot(pm, v_)
            upd = [_dot(v_[_rows(c)], ke_[_rows(c)], TN) for c in range(ncht)]
            st = state[h]
            for c in order:
                st_ref[c, h] = st
                st = st * et[c * CH:c * CH + 1, sl] + upd[c]
            state[h] = st
            inter = [_dot(qd_[_rows(c)], st_ref[c, h].astype(BF16), NT) for c in range(ncht)]
            outs.append(o_h + jnp.concatenate(inter, axis=0))
        o_ref[...] = jnp.concatenate(outs, axis=1)

    def col(cb):
        return pl.BlockSpec((TM, HGW), lambda s: (tile_of(s), cb))

    return _pcall(
        body, name="hgrn_fwd_rev" if rev else "hgrn_fwd", grid=(nt,),
        in_specs=[col(C_FB if rev else C_FF), col(C_INP), col(C_QHG),
                  pl.BlockSpec((1, 2, HGW), lambda s: (d, 0, 0))],
        out_specs=[pl.BlockSpec((TM, HGW), lambda s: (tile_of(s), 0)),
                   pl.BlockSpec((ncht, 4, HGD, HGD), lambda s: (tile_of(s), 0, 0, 0))],
        out_shape=[jax.ShapeDtypeStruct((tt, HGW), F32),
                   jax.ShapeDtypeStruct((nt * ncht, 4, HGD, HGD), F32)],
        scratch=[pltpu.VMEM((4, HGD, HGD), F32)], carry=carry)(p, p, p, lg)


def _hgrn_bwd(p, lg, do, st, dp, prev, *, rev, carry=None):
    tt = p.shape[0]
    nt = tt // TM
    ncht = TM // CH
    d = 1 if rev else 0
    second = prev is not None

    def tile_of(s):
        return jnp.where(s == nt - 1, 0, s + 1) if rev else nt - 1 - s

    def body(*refs):
        if second:
            (f_ref, inp_ref, q_ref, lg_ref, do_ref, st_ref, dvp_ref, dqp_ref, _dp_in,
             dp_ref, dlg_ref, dstate) = refs
        else:
            (f_ref, inp_ref, q_ref, lg_ref, do_ref, st_ref, _dp_in,
             dp_ref, dv_ref, dq_ref, dlg_ref, dstate) = refs
        s = pl.program_id(0)
        tile = tile_of(s)

        @pl.when(s == 0)
        def _():
            dstate[...] = jnp.zeros_like(dstate)
            dlg_ref[...] = jnp.zeros_like(dlg_ref)

        qraw = q_ref[...]
        lb, sg, f, q = _hgrn_gate(f_ref[...], qraw, lg_ref[0])
        lf = jnp.log(f)
        causal, _ = _chunk_masks(rev)
        causal_t, _ = _chunk_masks(rev, transpose=True)
        cum = _chunk_scan(lf, rev)
        tot = _chunk_total(lf)
        ea, eb, ee, et = jnp.exp(cum), jnp.exp(-cum), jnp.exp(tot - cum), jnp.exp(tot)
        qdf, kdf, kef = q * ea, (1.0 - f) * eb, (1.0 - f) * ee
        qd, kd, ke = qdf.astype(BF16), kdf.astype(BF16), kef.astype(BF16)
        v = inp_ref[...].astype(BF16)
        dob = jnp.where(tile == 0, 0.0, do_ref[...]).astype(BF16)
        order = range(ncht) if rev else range(ncht - 1, -1, -1)
        dq_l, dk_l, dv_l, dcum_l, dtot_l = [], [], [], [], []
        for h in range(4):
            sl = slice(h * HGD, (h + 1) * HGD)
            qd_, kd_, ke_, v_, do_ = qd[:, sl], kd[:, sl], ke[:, sl], v[:, sl], dob[:, sl]
            pmt = jnp.where(causal_t, _dot(kd_, qd_, NT), 0.0).astype(BF16)
            dpm = jnp.where(causal, _dot(do_, v_, NT), 0.0).astype(BF16)
            dpmt = jnp.where(causal_t, _dot(v_, do_, NT), 0.0).astype(BF16)
            dv = _dot(pmt, do_)
            dqd = _dot(dpm, kd_)
            dkd = _dot(dpmt, qd_)
            upd = [_dot(do_[_rows(c)], qd_[_rows(c)], TN) for c in range(ncht)]
            ds = dstate[h]
            ds1 = [None] * ncht
            for c in order:
                ds1[c] = ds
                ds = ds * et[c * CH:c * CH + 1, sl] + upd[c]
            dstate[h] = ds
            dke_c, dv_c, dqd_c, dtot_c = [], [], [], []
            for c in range(ncht):
                st0 = st_ref[c, h]
                dsb = ds1[c].astype(BF16)
                dke_ = _dot(v_[_rows(c)], dsb)
                dke_c.append(dke_)
                dv_c.append(_dot(ke_[_rows(c)], dsb, NT))
                dqd_c.append(_dot(do_[_rows(c)], st0.astype(BF16)))
                dt = (jnp.sum(ds1[c] * st0, axis=0, keepdims=True) * et[c * CH:c * CH + 1, sl]
                      + jnp.sum(dke_ * kef[_rows(c), sl], axis=0, keepdims=True))
                dtot_c.append(jnp.broadcast_to(dt, (CH, HGD)))
            dke = jnp.concatenate(dke_c, axis=0)
            dqd = dqd + jnp.concatenate(dqd_c, axis=0)
            dv_l.append(dv + jnp.concatenate(dv_c, axis=0))
            dtot_l.append(jnp.concatenate(dtot_c, axis=0))
            dq_l.append(dqd * ea[:, sl])
            dk_l.append(dkd * eb[:, sl] + dke * ee[:, sl])
            dcum_l.append(dqd * qdf[:, sl] - dkd * kdf[:, sl] - dke * kef[:, sl])
        dcum = jnp.concatenate(dcum_l, axis=1)
        dlf = _chunk_scan(dcum, rev, transpose=True) + jnp.concatenate(dtot_l, axis=1)
        dq_t = jnp.concatenate(dq_l, axis=1)
        dv_t = jnp.concatenate(dv_l, axis=1)

        df = dlf / f - jnp.concatenate(dk_l, axis=1)
        dfl = df * (1.0 - lb) * sg * (1.0 - sg)
        dlb = jnp.sum(df * (1.0 - sg), axis=0, keepdims=True)
        dl0 = dlb * lb * (1.0 - lb)
        dlg_ref[...] += jnp.concatenate([dl0, -dl0], axis=0)[None]
        if second:
            sq = _sig(qraw)
            dqr = (dqp_ref[...] + dq_t) * (HGD ** -0.5) * (sq * (1.0 + qraw * (1.0 - sq)))
            dp_ref[...] = jnp.concatenate([dfl, dvp_ref[...] + dv_t, dqr], axis=1).astype(BF16)
        else:
            dp_ref[...] = dfl.astype(BF16)
            dv_ref[...] = dv_t
            dq_ref[...] = dq_t

    def col(cb):
        return pl.BlockSpec((TM, HGW), lambda s: (tile_of(s), cb))

    tok = pl.BlockSpec((TM, HGW), lambda s: (tile_of(s), 0))
    in_specs = [col(C_FB if rev else C_FF), col(C_INP), col(C_QHG),
                pl.BlockSpec((1, 2, HGW), lambda s: (d, 0, 0)),
                pl.BlockSpec((TM, HGW), lambda s: (jnp.maximum(tile_of(s) - 1, 0), 0)),
                pl.BlockSpec((ncht, 4, HGD, HGD), lambda s: (tile_of(s), 0, 0, 0))]
    args = [p, p, p, lg, do, st]
    dlg_spec = _full((1, 2, HGW))
    dlg_shape = jax.ShapeDtypeStruct((1, 2, HGW), F32)
    if second:
        in_specs += [tok, tok]
        args += [prev[0], prev[1]]
        out_specs = [pl.BlockSpec((TM, 3 * HGW), lambda s: (tile_of(s), 0)), dlg_spec]
        out_shape = [jax.ShapeDtypeStruct(dp.shape, BF16), dlg_shape]
    else:
        out_specs = [pl.BlockSpec((TM, HGW), lambda s: (tile_of(s), C_FB if rev else C_FF)), tok, tok, dlg_spec]
        out_shape = [jax.ShapeDtypeStruct(dp.shape, BF16), jax.ShapeDtypeStruct((tt, HGW), F32),
                     jax.ShapeDtypeStruct((tt, HGW), F32), dlg_shape]
    in_specs.append(ANY)
    args.append(dp)
    return _pcall(body, name="hgrn_bwd_rev" if rev else "hgrn_bwd", grid=(nt,),
                  in_specs=in_specs, out_specs=out_specs, out_shape=out_shape,
                  scratch=[pltpu.VMEM((4, HGD, HGD), F32)],
                  aliases={len(args) - 1: 0}, carry=carry)(*args)


def _head_rms(o, w, nheads):
    outs = []
    for h in range(nheads):
        oh = o[:, h * HGD:(h + 1) * HGD]
        outs.append(oh * lax.rsqrt(jnp.mean(oh * oh, axis=-1, keepdims=True) + EPS))
    return jnp.concatenate(outs, axis=1)


def _readout(o0, o1, p, hw4):
    s_len = o0.shape[0] - L

    def body(o0_ref, o1_ref, g_ref, w_ref, y_ref):
        xh = _head_rms(o0_ref[...] + o1_ref[...], None, 4)
        g = g_ref[...]
        y_ref[...] = (xh * w_ref[...] * (g * _sig(g))).astype(BF16)

    lat = pl.BlockSpec((TM, HGW), lambda i: (i + 1, 0))
    return _pcall(body, name="readout", grid=(s_len // TM,),
                  in_specs=[lat, lat, pl.BlockSpec((TM, HGW), lambda i: (i + 1, C_GHG)), _full((1, HGW))],
                  out_specs=pl.BlockSpec((TM, HGW), lambda i: (i, 0)),
                  out_shape=jax.ShapeDtypeStruct((s_len, HGW), BF16))(o0, o1, p, hw4)


def _readout_bwd(o0, o1, p, hw4, dy, dp, carry=None):
    tt = o0.shape[0]
    s_len = tt - L

    def body(o0_ref, o1_ref, g_ref, w_ref, dy_ref, _dp_in, dp_ref, do_ref, dw_ref):
        i = pl.program_id(0)

        @pl.when(i == 0)
        def _():
            dw_ref[...] = jnp.zeros_like(dw_ref)
            dp_ref[...] = jnp.zeros_like(dp_ref)

        @pl.when(i >= 1)
        def _():
            o = o0_ref[...] + o1_ref[...]
            g = g_ref[...]
            w = w_ref[...]
            sg = _sig(g)
            dy_ = dy_ref[...]
            dsw = dy_ * (g * sg)
            outs, xhs = [], []
            for h in range(4):
                sl = slice(h * HGD, (h + 1) * HGD)
                oh = o[:, sl]
                r = lax.rsqrt(jnp.mean(oh * oh, axis=-1, keepdims=True) + EPS)
                xh = oh * r
                dxh = dsw[:, sl] * w[:, sl]
                outs.append(r * (dxh - xh * jnp.mean(dxh * xh, axis=-1, keepdims=True)))
                xhs.append(xh)
            xh = jnp.concatenate(xhs, axis=1)
            do_ref[...] = jnp.concatenate(outs, axis=1)
            dp_ref[...] = (dy_ * xh * w * (sg * (1.0 + g * (1.0 - sg)))).astype(BF16)
            dw_ref[...] += jnp.sum(dsw * xh, axis=0, keepdims=True)

    tok = pl.BlockSpec((TM, HGW), lambda i: (i, 0))
    lat = pl.BlockSpec((TM, HGW), lambda i: (jnp.maximum(i - 1, 0), 0))
    return _pcall(body, name="readout_bwd", grid=(tt // TM,),
                  in_specs=[tok, tok, pl.BlockSpec((TM, HGW), lambda i: (i, C_GHG)), _full((1, HGW)), lat, ANY],
                  out_specs=[pl.BlockSpec((TM, HGW), lambda i: (i, C_GHG)), lat, _full((1, HGW))],
                  out_shape=[jax.ShapeDtypeStruct(dp.shape, BF16), jax.ShapeDtypeStruct((s_len, HGW), F32),
                             jax.ShapeDtypeStruct((1, HGW), F32)],
                  aliases={5: 0}, carry=carry)(o0, o1, p, hw4, dy, dp)


def _rope_tables(s_len):
    t = np.arange(s_len)
    inv = ROPE_THETA ** (-np.arange(0, 32, 2, dtype=np.float64) / 32)
    def half(pos):
        ang = pos[:, None].astype(np.float64) * inv[None, :]
        return (np.concatenate([np.cos(ang), np.cos(ang)], 1), np.concatenate([-np.sin(ang), np.sin(ang)], 1))
    cr, sr = half(t // GRID_W)
    cc, sc = half(t % GRID_W)
    cos = np.concatenate([cr, cc, cr, cc], 1)
    sin = np.concatenate([sr, sc, sr, sc], 1)
    cos = np.concatenate([np.ones((L, 128)), cos], 0)
    sin = np.concatenate([np.zeros((L, 128)), sin], 0)
    return jnp.asarray(cos, F32), jnp.asarray(sin, F32)


def _blockdiag(n, w):
    i = np.arange(n)
    return jnp.asarray((i[:, None] // w == i[None, :] // w) / float(w), F32)


def _dup_matrix():
    m = np.zeros((128, 512), np.float32)
    for g in range(2):
        for j in range(4):
            for dd in range(HDIM):
                m[64 * g + dd, 256 * g + 64 * j + dd] = 1.0
    return m


def _head_mean(x, blockdiag):
    return _dot(x, blockdiag, prec=lax.Precision.HIGH)


def _rot(x):
    n = x.shape[1]
    lane = lax.broadcasted_iota(jnp.int32, x.shape, 1)
    return jnp.where((lane % 32) < 16, pltpu.roll(x, n - 16, 1), pltpu.roll(x, 16, 1))


def _qk_prep(p, cos, sin, qnw8, knw2, bd512, bd128, dup):
    tt = p.shape[0]

    def body(q_ref, kv_ref, cos_ref, sin_ref, qw_ref, kw_ref, b5_ref, b1_ref, dup_ref,
             qr_ref, k4_ref, v4_ref):
        cos_, sin_ = cos_ref[...], sin_ref[...]
        q = q_ref[...]
        qn = q * lax.rsqrt(_head_mean(q * q, b5_ref[...]) + EPS) * qw_ref[...]
        cos4 = jnp.concatenate([cos_] * 4, axis=1)
        sin4 = jnp.concatenate([sin_] * 4, axis=1)
        qr_ref[...] = ((qn * cos4 + _rot(qn) * sin4) * (HDIM ** -0.5)).astype(BF16)
        kv = kv_ref[...]
        k, v = kv[:, :128], kv[:, 128:]
        kn = k * lax.rsqrt(_head_mean(k * k, b1_ref[...]) + EPS) * kw_ref[...]
        kr = kn * cos_ + _rot(kn) * sin_
        k4_ref[...] = _bdot(kr, dup_ref[...]).astype(BF16)
        v4_ref[...] = _bdot(v, dup_ref[...]).astype(BF16)

    row = lambda w, cb: pl.BlockSpec((TM, w), lambda i: (i, cb))
    out = jax.ShapeDtypeStruct((tt, ATW), BF16)
    return _pcall(body, name="qk_prep", grid=(tt // TM,),
                  in_specs=[row(ATW, C_QRAW), row(256, C_KV), row(128, 0), row(128, 0),
                            _full((1, ATW)), _full((1, 128)), _full((ATW, ATW)), _full((128, 128)),
                            _full((128, ATW))],
                  out_specs=[row(ATW, 0)] * 3, out_shape=[out] * 3)(
                      p, p, cos, sin, qnw8, knw2, bd512, bd128, dup)


def _attn_masks(i, nb):
    r = lax.broadcasted_iota(jnp.int32, (4 * BLK, 3 * BLK + L), 0) % BLK
    c = lax.broadcasted_iota(jnp.int32, (4 * BLK, 3 * BLK + L), 1)
    kpos = (i - 1) * BLK + c
    loc = (jnp.abs(c - BLK - r) <= BLK) & (kpos >= 0) & (kpos < nb * BLK)
    return loc | (c >= 3 * BLK)


def _stack_mask():
    r = lax.broadcasted_iota(jnp.int32, (4 * BLK, 256), 0)
    lane = lax.broadcasted_iota(jnp.int32, (4 * BLK, 256), 1)
    return (r // BLK) == (lane // HDIM)


def _stack_heads(xg, fill=0.0):
    x4 = jnp.concatenate([xg] * 4, axis=0)
    return jnp.where(_stack_mask(), x4, jnp.full_like(x4, fill))


def _unstack_heads(x4):
    out = jnp.where(_lane_mask(0), x4[0:BLK], 0.0)
    for j in range(1, 4):
        out = out + jnp.where(_lane_mask(j), x4[j * BLK:(j + 1) * BLK], 0.0)
    return out


def _per_head_rows(vals):
    return jnp.concatenate([jnp.broadcast_to(v, (BLK, 1)) for v in vals], axis=0)


def _lane_mask(j):
    lane = lax.broadcasted_iota(jnp.int32, (1, 256), 1)
    return (lane // HDIM) == j


def _attn_specs(nb):
    blk = lambda off: pl.BlockSpec((BLK, ATW), lambda i: (jnp.clip(i + off, 0, nb - 1) + 2, 0))
    ctx = pl.BlockSpec((L, ATW), lambda i: (0, 0))
    return blk, ctx


def _attn_fwd(qr, k4, v4, sinks, carry=None):
    tt = qr.shape[0]
    s_len = tt - L
    nb = s_len // BLK

    def body(sk_ref, q_ref, kp, ko, kn, kc, vp, vo, vn, vc, y_ref, lse_ref):
        i = pl.program_id(0)
        valid = _attn_masks(i, nb)
        q = q_ref[...]
        ys, lses = [], []
        for g in range(2):
            gs = slice(256 * g, 256 * g + 256)
            kcat = jnp.concatenate([kp[:, gs], ko[:, gs], kn[:, gs], kc[:, gs]], axis=0)
            vcat = jnp.concatenate([vp[:, gs], vo[:, gs], vn[:, gs], vc[:, gs]], axis=0)
            sink = _per_head_rows([sk_ref[4 * g + j] for j in range(4)])
            s = jnp.where(valid, _dot(_stack_heads(q[:, gs]), kcat, NT), -1e30)
            m = jnp.maximum(jnp.max(s, axis=-1, keepdims=True), sink)
            e = jnp.exp(s - m)
            den = jnp.sum(e, axis=-1, keepdims=True) + jnp.exp(sink - m)
            ys.append(_unstack_heads(_bdot(e * (1.0 / den), vcat)))
            lses.append(_unstack_heads(jnp.broadcast_to(m + jnp.log(den), (4 * BLK, 256))))
        y_ref[...] = jnp.concatenate(ys, axis=1).astype(BF16)
        lse_ref[...] = jnp.concatenate(lses, axis=1)

    blk, ctx = _attn_specs(nb)
    out = pl.BlockSpec((BLK, ATW), lambda i: (i, 0))
    return _pcall(body, name="attn_fwd", grid=(nb,),
                  in_specs=[pl.BlockSpec(memory_space=pltpu.SMEM), blk(0),
                            blk(-1), blk(0), blk(1), ctx, blk(-1), blk(0), blk(1), ctx],
                  out_specs=[out, out],
                  out_shape=[jax.ShapeDtypeStruct((s_len, ATW), BF16),
                             jax.ShapeDtypeStruct((s_len, ATW), F32)], carry=carry)(
                      sinks, qr, k4, k4, k4, k4, v4, v4, v4, v4)


def _attn_bwd(qr, k4, v4, sinks, y, lse, dy, carry=None):
    tt = qr.shape[0]
    s_len = tt - L
    nb = s_len // BLK

    def body(sk_ref, q_ref, kp, ko, kn, kc, vp, vo, vn, vc, y_ref, lse_ref, dy_ref,
             dq_ref, dkw_ref, dvw_ref, dkc_ref, dvc_ref, dsk_ref):
        i = pl.program_id(0)

        @pl.when(i == 0)
        def _():
            dkc_ref[...] = jnp.zeros_like(dkc_ref)
            dvc_ref[...] = jnp.zeros_like(dvc_ref)
            dsk_ref[...] = jnp.zeros_like(dsk_ref)

        valid = _attn_masks(i, nb)
        q = q_ref[...]
        dy_ = dy_ref[...]
        dly = dy_ * y_ref[...].astype(F32)
        lse_ = lse_ref[...]
        dqs = []
        for g in range(2):
            gs = slice(256 * g, 256 * g + 256)
            kcat = jnp.concatenate([kp[:, gs], ko[:, gs], kn[:, gs], kc[:, gs]], axis=0)
            vcat = jnp.concatenate([vp[:, gs], vo[:, gs], vn[:, gs], vc[:, gs]], axis=0)
            q4 = _stack_heads(q[:, gs])
            dy4 = _stack_heads(dy_[:, gs]).astype(BF16)
            lse4 = jnp.max(_stack_heads(lse_[:, gs], fill=-1e30), axis=-1, keepdims=True)
            delta = jnp.sum(_stack_heads(dly[:, gs]), axis=-1, keepdims=True)
            sink = _per_head_rows([sk_ref[4 * g + j] for j in range(4)])
            pr = jnp.where(valid, jnp.exp(_dot(q4, kcat, NT) - lse4), 0.0)
            dsb = (pr * (_dot(dy4, vcat, NT) - delta)).astype(BF16)
            dsink = jnp.exp(sink - lse4) * delta
            for j in range(4):
                dsk_ref[4 * g + j:4 * g + j + 1, :] += jnp.broadcast_to(
                    -jnp.sum(dsink[j * BLK:(j + 1) * BLK], axis=0, keepdims=True), (1, 128))
            dqs.append(_unstack_heads(_dot(dsb, kcat)))
            dkg = _dot(dsb, q4, TN)
            dvg = _dot(pr.astype(BF16), dy4, TN)
            dkw_ref[0, :, gs] = dkg[:3 * BLK]
            dvw_ref[0, :, gs] = dvg[:3 * BLK]
            dkc_ref[:, gs] += dkg[3 * BLK:]
            dvc_ref[:, gs] += dvg[3 * BLK:]
        dq_ref[...] = jnp.concatenate(dqs, axis=1)

    blk, ctx = _attn_specs(nb)
    out = pl.BlockSpec((BLK, ATW), lambda i: (i, 0))
    win = pl.BlockSpec((1, 3 * BLK, ATW), lambda i: (i, 0, 0))
    acc = _full((L, ATW))
    return _pcall(body, name="attn_bwd", grid=(nb,),
                  in_specs=[pl.BlockSpec(memory_space=pltpu.SMEM), blk(0),
                            blk(-1), blk(0), blk(1), ctx, blk(-1), blk(0), blk(1), ctx, out, out, out],
                  out_specs=[out, win, win, acc, acc, _full((8, 128))],
                  out_shape=[jax.ShapeDtypeStruct((s_len, ATW), F32),
                             jax.ShapeDtypeStruct((nb, 3 * BLK, ATW), F32),
                             jax.ShapeDtypeStruct((nb, 3 * BLK, ATW), F32),
                             jax.ShapeDtypeStruct((L, ATW), F32), jax.ShapeDtypeStruct((L, ATW), F32),
                             jax.ShapeDtypeStruct((8, 128), F32)], carry=carry)(
                      sinks, qr, k4, k4, k4, k4, v4, v4, v4, v4, y, lse, dy)


def _attn_post(p, cos, sin, qnw8, knw2, bd512, bd128, dupt, dq, dkw, dvw, dkc, dvc, dp, carry=None):
    tt = p.shape[0]
    s_len = tt - L
    nb = s_len // BLK
    nctx = L // BLK

    def body(q_ref, kv_ref, cos_ref, sin_ref, qw_ref, kw_ref, b5_ref, b1_ref, dupt_ref,
             dq_ref, kwp, kwo, kwn, vwp, vwo, vwn, dkc_ref, dvc_ref, _dp_in,
             dp_ref, dqw_ref, dkw_ref):
        t = pl.program_id(0)
        j = t - nctx

        @pl.when(t == 0)
        def _():
            dqw_ref[...] = jnp.zeros_like(dqw_ref)
            dkw_ref[...] = jnp.zeros_like(dkw_ref)

        is_lat = t >= nctx
        cos_, sin_ = cos_ref[...], sin_ref[...]
        has_p = is_lat & (j >= 1)
        has_n = is_lat & (j <= nb - 2)
        dk4 = (jnp.where(is_lat, kwo[0], dkc_ref[...]) + jnp.where(has_p, kwp[0], 0.0)
               + jnp.where(has_n, kwn[0], 0.0))
        dv4 = (jnp.where(is_lat, vwo[0], dvc_ref[...]) + jnp.where(has_p, vwp[0], 0.0)
               + jnp.where(has_n, vwn[0], 0.0))
        dkr = _dot(dk4, dupt_ref[...], prec=HI)
        dv = _dot(dv4, dupt_ref[...], prec=HI)
        kv = kv_ref[...]
        k = kv[:, :128]
        kw = kw_ref[...]
        rk = lax.rsqrt(_head_mean(k * k, b1_ref[...]) + EPS)
        xk = k * rk
        dkn = dkr * cos_ + _rot(dkr * sin_)
        dxk = dkn * kw
        dk = rk * (dxk - xk * _head_mean(dxk * xk, b1_ref[...]))
        dkw_ref[...] += jnp.sum(dkn * xk, axis=0, keepdims=True)
        q = q_ref[...]
        qw = qw_ref[...]
        rq = lax.rsqrt(_head_mean(q * q, b5_ref[...]) + EPS)
        xq = q * rq
        cos4 = jnp.concatenate([cos_] * 4, axis=1)
        sin4 = jnp.concatenate([sin_] * 4, axis=1)
        dqr = jnp.where(is_lat, dq_ref[...], 0.0) * (HDIM ** -0.5)
        dqn = dqr * cos4 + _rot(dqr * sin4)
        dxq = dqn * qw
        dqraw = rq * (dxq - xq * _head_mean(dxq * xq, b5_ref[...]))
        dqw_ref[...] += jnp.sum(dqn * xq, axis=0, keepdims=True)
        dp_ref[...] = jnp.concatenate([dqraw, dk, dv], axis=1).astype(BF16)

    row = lambda w, cb: pl.BlockSpec((BLK, w), lambda t: (t, cb))
    lat = pl.BlockSpec((BLK, ATW), lambda t: (jnp.maximum(t - nctx, 0), 0))

    def part(off):
        return pl.BlockSpec((1, BLK, ATW), lambda t: (jnp.clip(t - nctx + off, 0, nb - 1), 1 - off, 0))

    cacc = pl.BlockSpec((BLK, ATW), lambda t: (jnp.minimum(t, nctx - 1), 0))
    return _pcall(body, name="attn_post", grid=(tt // BLK,),
                  in_specs=[row(ATW, C_QRAW), row(256, C_KV), row(128, 0), row(128, 0),
                            _full((1, ATW)), _full((1, 128)), _full((ATW, ATW)), _full((128, 128)),
                            _full((ATW, 128)), lat, part(-1), part(0), part(1), part(-1), part(0), part(1),
                            cacc, cacc, ANY],
                  out_specs=[pl.BlockSpec((BLK, 768), lambda t: (t, C_QKV)), _full((1, ATW)), _full((1, 128))],
                  out_shape=[jax.ShapeDtypeStruct(dp.shape, BF16), jax.ShapeDtypeStruct((1, ATW), F32),
                             jax.ShapeDtypeStruct((1, 128), F32)],
                  aliases={18: 0}, carry=carry)(p, p, cos, sin, qnw8, knw2, bd512, bd128, dupt,
                                   dq, dkw, dkw, dkw, dvw, dvw, dvw, dkc, dvc, dp)


def _branch_merge(y_hg, y_at, bh4, ba4, p):
    s_len = y_hg.shape[0]

    def body(yh_ref, ya_ref, bh_ref, ba_ref, gh_ref, ga_ref, ah_ref, aa_ref, m_ref):
        yh, ya = yh_ref[...], ya_ref[...]
        ah = jnp.concatenate([_bdot(yh, bh_ref[j]) for j in range(4)], axis=1)
        aa = jnp.concatenate([_bdot(ya, ba_ref[j]) for j in range(4)], axis=1)
        ah_ref[...] = ah
        aa_ref[...] = aa
        m_ref[...] = (_sig(gh_ref[...]) * ah + _sig(ga_ref[...]) * aa).astype(BF16)

    row = pl.BlockSpec((TM, D), lambda i: (i, 0))
    y = pl.BlockSpec((TM, HGW), lambda i: (i, 0))
    f = jax.ShapeDtypeStruct((s_len, D), F32)
    return _pcall(body, name="branch_merge", grid=(s_len // TM,),
                  in_specs=[y, y, _full(bh4.shape), _full(ba4.shape),
                            pl.BlockSpec((TM, D), lambda i: (i + 1, 2)), pl.BlockSpec((TM, D), lambda i: (i + 1, 3))],
                  out_specs=[row, row, row],
                  out_shape=[f, f, jax.ShapeDtypeStruct((s_len, D), BF16)])(y_hg, y_at, bh4, ba4, p, p)


def _branch_bwd(dmh, dma, bh4, ba4, y_hg, y_at):
    s_len = dmh.shape[0]
    nk = s_len // TS
    ns = D // 4

    def body(dh_ref, da_ref, bh_ref, ba_ref, yh_ref, ya_ref, dyh_ref, dya_ref, gh_ref, ga_ref, acc_h, acc_a):
        t = pl.program_id(0)

        @pl.when(t == 0)
        def _():
            acc_h[...] = jnp.zeros_like(acc_h)
            acc_a[...] = jnp.zeros_like(acc_a)

        for d_ref, w_ref, y_ref, dy_ref, acc in ((dh_ref, bh_ref, yh_ref, dyh_ref, acc_h),
                                                 (da_ref, ba_ref, ya_ref, dya_ref, acc_a)):
            y = y_ref[...]
            dy = jnp.zeros((TS, HGW), F32)
            for j in range(4):
                dj = d_ref[:, j * ns:(j + 1) * ns]
                dy = dy + _bdot(dj, w_ref[j], NT)
                acc[j] += _bdot(y, dj, TN)
            dy_ref[...] = dy

        @pl.when(t == nk - 1)
        def _():
            gh_ref[...] = acc_h[...].astype(BF16)
            ga_ref[...] = acc_a[...].astype(BF16)

    dm = pl.BlockSpec((TS, D), lambda t: (t, 0))
    y = pl.BlockSpec((TS, HGW), lambda t: (t, 0))
    w = _full(bh4.shape)
    fy = jax.ShapeDtypeStruct((s_len, HGW), F32)
    gw = jax.ShapeDtypeStruct(bh4.shape, BF16)
    return _pcall(body, name="branch_bwd", grid=(nk,), in_specs=[dm, dm, w, w, y, y],
                  out_specs=[y, y, w, w], out_shape=[fy, fy, gw, gw],
                  scratch=[pltpu.VMEM(bh4.shape, F32)] * 2)(dmh, dma, bh4, ba4, y_hg, y_at)


def _merge_bwd(dattn, w_o, mixed, ah, aa, p, carry=None):
    tt = p.shape[0]
    s_len = tt - L
    nt = tt // TM

    def body(da_ref, wo_ref, mx_ref, ah_ref, aa_ref, gh_ref, ga_ref, dp_ref, dmh_ref, dma_ref, go_ref, acc):
        i = pl.program_id(0)

        @pl.when(i == 0)
        def _():
            dp_ref[...] = jnp.zeros_like(dp_ref)
            acc[...] = jnp.zeros_like(acc)

        @pl.when(i >= 1)
        def _():
            da = da_ref[...]
            acc[...] += _bdot(mx_ref[...], da, TN)
            dm_ = _bdot(da, wo_ref[...], NT)
            sh, sa = _sig(gh_ref[...]), _sig(ga_ref[...])
            dp_ref[...] = jnp.concatenate([dm_ * ah_ref[...] * sh * (1.0 - sh),
                                           dm_ * aa_ref[...] * sa * (1.0 - sa)], axis=1).astype(BF16)
            dmh_ref[...] = (dm_ * sh).astype(BF16)
            dma_ref[...] = (dm_ * sa).astype(BF16)

        @pl.when(i == nt - 1)
        def _():
            go_ref[...] = acc[...].astype(BF16)

    lat = pl.BlockSpec((TM, D), lambda i: (jnp.maximum(i - 1, 0), 0))
    return _pcall(body, name="merge_bwd", grid=(nt,),
                  in_specs=[lat, _full((D, D)), lat, lat, lat, pl.BlockSpec((TM, D), lambda i: (i, 2)),
                            pl.BlockSpec((TM, D), lambda i: (i, 3))],
                  out_specs=[pl.BlockSpec((TM, 2 * D), lambda i: (i, C_GATES)), lat, lat, _full((D, D))],
                  out_shape=[jax.ShapeDtypeStruct((tt, NCOL), BF16), jax.ShapeDtypeStruct((s_len, D), BF16),
                             jax.ShapeDtypeStruct((s_len, D), BF16), jax.ShapeDtypeStruct((D, D), BF16)],
                  scratch=[pltpu.VMEM((D, D), F32)], carry=carry)(dattn, w_o, mixed, ah, aa, p, p)


def _local_step(x, ctx, tgt, mod, modc, nw1, nw2, lg, hw, qnw, knw, sinks,
                w_in, wts, dist=None):
    s_len = x.shape[0]
    tt = s_len + L
    ss1 = jnp.stack([modc, mod[0:2]])
    ss2 = mod[3:5][None]
    g1, g2 = mod[2:3], mod[5:6]
    hw4 = jnp.tile(hw, (1, 4))
    qnw8 = jnp.tile(qnw, (1, 8))
    knw2 = jnp.tile(knw, (1, 2))
    cos, sin = _rope_tables(s_len)
    bd512, bd128 = _blockdiag(ATW, HDIM), _blockdiag(128, HDIM)
    dupm = _dup_matrix()
    dup, dupt = jnp.asarray(dupm, BF16), jnp.asarray(dupm.T, F32)
    tmt = tt

    def four(b):
        return b.reshape(4, 2 * b.shape[1], b.shape[2])

    def halves(g):
        return g.reshape(4, 2, g.shape[1] // 2, g.shape[2])

    h = _mod1(ctx, x, nw1, ss1)
    if dist is None:
        bh4, ba4, w_o, g4, u4, dn4 = wts
        p = _mm_in(h, w_in, tmt)
        o0, st0 = _hgrn_fwd(p, lg, rev=False)
        o1, st1 = _hgrn_fwd(p, lg, rev=True)
    else:
        core, chip = dist
        half = wts[3].shape[1] // 2
        p, first = _mm_in(h, w_in, tmt, carry=_carry_join(_carry_gather(list(wts[0:3])),
                                                          _carry_gather([wts[3]], rows=[(0, half)])))
        (o0, st0), (g8,) = _hgrn_fwd(p, lg, rev=False, carry=_carry_gather([first[3]], rows=[(half, half)]))
        o1, st1 = _hgrn_fwd(p, lg, rev=True)
        bh4, ba4, w_o, g4 = four(first[0]), four(first[1]), four(first[2]).reshape(D, D), four(g8)
    y_hg = _readout(o0, o1, p, hw4)
    qr, k4, v4 = _qk_prep(p, cos, sin, qnw8, knw2, bd512, bd128, dup)
    if dist is None:
        y_at, lse = _attn_fwd(qr, k4, v4, sinks)
    else:
        (y_at, lse), (u8,) = _attn_fwd(qr, k4, v4, sinks, carry=_carry_gather([wts[4]]))
        u4 = four(u8)
    ah, aa, mixed = _branch_merge(y_hg, y_at, bh4, ba4, p)
    ao, x1, h2 = _out_proj_mod2(mixed, w_o, x, g1, nw2, ss2)
    if dist is None:
        a4, b4, z4 = _ffn_up(h2, g4, u4)
    else:
        (a4, b4, z4), (dn8,) = _ffn_up(h2, g4, u4, carry=_carry_gather([wts[5]]))
        dn4 = four(dn8)
    sq, dx2, dyb, dg2 = _ffn_down_loss(z4, dn4, x1, g2, tgt)

    da4, db4 = _ffn_dz(dyb, dn4, a4, b4)
    g_dn = _ffn_gdn(z4, dyb)
    if dist is None:
        dh2 = _ffn_dh2(da4, db4, g4, u4)
    else:
        dn_units = [halves(g_dn)]
        dh2, dn_recv = _ffn_dh2(da4, db4, g4, u4, carry=_carry_pairx(dn_units))
        dn_pairs = _rs_pair_add(dn_units, dn_recv, core)
    if dist is None:
        g_g, g_u = _ffn_ggu(h2, da4, db4)
    else:
        (g_g, g_u), c_dn = _ffn_ggu(h2, da4, db4, carry=_carry_chipx(dn_pairs))
        red_dn = _rs_chip_add(dn_pairs, c_dn, core, chip)
    dx1, dattn, dss2, dnw2, dg1 = _mod2_bwd(x1, dh2, dx2, ao, nw2, ss2, g1)
    if dist is None:
        dp, dmh, dma, g_o = _merge_bwd(dattn, w_o, mixed, ah, aa, p)
    else:
        gu_units = [halves(g_g), halves(g_u)]
        (dp, dmh, dma, g_o), gu_recv = _merge_bwd(dattn, w_o, mixed, ah, aa, p, carry=_carry_pairx(gu_units))
        ffn_pairs = list(dn_pairs) + list(_rs_pair_add(gu_units, gu_recv, core))
    dy_hg, dy_at, g_bh, g_ba = _branch_bwd(dmh, dma, bh4, ba4, y_hg, y_at)
    if dist is None:
        dp, do, dhw4 = _readout_bwd(o0, o1, p, hw4, dy_hg, dp)
        dq, dkw, dvw, dkc, dvc, dsk = _attn_bwd(qr, k4, v4, sinks, y_at, lse, dy_at)
        dp, dqnw8, dknw2 = _attn_post(p, cos, sin, qnw8, knw2, bd512, bd128, dupt, dq, dkw, dvw, dkc, dvc, dp)
    else:
        mix_units = [halves(g_bh), halves(g_ba), halves(g_o.reshape(4, D // 4, D))]
        (dp, do, dhw4), mix_recv = _readout_bwd(o0, o1, p, hw4, dy_hg, dp, carry=_carry_pairx(mix_units))
        mix_pairs = _rs_pair_add(mix_units, mix_recv, core)
        (dq, dkw, dvw, dkc, dvc, dsk), bwd = _attn_bwd(
            qr, k4, v4, sinks, y_at, lse, dy_at,
            carry=_carry_join(_carry_chipx(ffn_pairs[1:2]), _carry_sibx(red_dn)))
        red_g = _rs_chip_add(ffn_pairs[1:2], bwd[0:1], core, chip)
        (dp, dqnw8, dknw2), post = _attn_post(
            p, cos, sin, qnw8, knw2, bd512, bd128, dupt, dq, dkw, dvw, dkc, dvc, dp,
            carry=_carry_join(_carry_chipx(ffn_pairs[2:3]), _carry_sibx(red_g)))
        red_u = _rs_chip_add(ffn_pairs[2:3], post[0:1], core, chip)
    if dist is None:
        dp, dv0, dq0, dlg0 = _hgrn_bwd(p, lg, do, st0, dp, None, rev=False)
        dp, dlg1 = _hgrn_bwd(p, lg, do, st1, dp, (dv0, dq0), rev=True)
    else:
        (dp, dv0, dq0, dlg0), mid = _hgrn_bwd(p, lg, do, st0, dp, None, rev=False,
                                              carry=_carry_join(_carry_chipx(mix_pairs), _carry_sibx(red_u)))
        mix_reds = _rs_chip_add(mix_pairs, mid[0:3], core, chip)
        (dp, dlg1), mix_done = _hgrn_bwd(p, lg, do, st1, dp, (dv0, dq0), rev=True, carry=_carry_sibx(mix_reds))
        ffn_done = bwd[1:2] + post[1:2] + mid[3:4]
    g_in = _mm_gin(dp, h, tmt)
    if dist is None:
        dh = _mm_dh(dp, w_in, tmt)
        gx, dss1, dnw1 = _mod1_bwd(ctx, x, dh, dx1, nw1, ss1)
        rs = None
    else:
        in_units = [halves(g_in.reshape(4, NCOL // 4, D))]
        dh, in_recv = _mm_dh(dp, w_in, tmt, carry=_carry_pairx(in_units))
        in_pairs = _rs_pair_add(in_units, in_recv, core)
        first_rows = (0, in_pairs[0].shape[1] // 2)
        (gx, dss1, dnw1), in_part = _mod1_bwd(ctx, x, dh, dx1, nw1, ss1,
                                              carry=_carry_chipx(in_pairs, rows=first_rows))
        rs = dict(ffn_done=ffn_done, mix_done=mix_done, in_pairs=in_pairs, in_part=in_part)

    dmod = jnp.concatenate([dss1[1], dg1, dss2, dg2], axis=0)
    dmodc = dss1[0]
    raw = (dss1, dg1, dss2, dg2, dnw1, dnw2, dhw4, dqnw8, dknw2, dsk, dlg0, dlg1)
    small = dict(raw=raw, dmod=dmod, dmodc=dmodc, dnw1=dnw1, dnw2=dnw2,
                 dhw=dhw4.reshape(4, HGD).sum(0, keepdims=True),
                 dqnw=dqnw8.reshape(8, HDIM).sum(0, keepdims=True),
                 dknw=dknw2.reshape(2, HDIM).sum(0, keepdims=True),
                 dsinks=dsk[:, 0], dlg=jnp.concatenate([dlg0, dlg1], axis=0))
    big = dict(w_in=g_in, w_bh=g_bh, w_ba=g_ba, w_o=g_o, w_g=g_g, w_u=g_u, w_dn=g_dn)
    return sq, gx, big, small, rs


def _place():
    x, y, c = lax.axis_index("x"), lax.axis_index("y"), lax.axis_index("c")
    return x, y, c


def _gather_blocks(x_refs, out_refs, send_sems, recv_sems, local_sems):
    n = len(out_refs)
    x, y, c = _place()
    me, sibling = (x, y, c), (x, y, 1 - c)
    chips = [(1 - x, y), (x, 1 - y), (1 - x, 1 - y)]

    def slot(u, px, py, pc):
        return out_refs[u].at[4 * px + 2 * py + pc]

    def copy(u, k, block, to, src=None):
        return pltpu.make_async_remote_copy(
            src_ref=slot(u, *block) if src is None else src, dst_ref=slot(u, *block),
            send_sem=send_sems.at[u, k], recv_sem=recv_sems.at[u, k], device_id=to, device_id_type=MESH)

    mines = []
    if x_refs is not None:
        mines = [pltpu.make_async_copy(x_refs[u], slot(u, *me), local_sems.at[u]) for u in range(n)]
    for cp in mines:
        cp.start()
    first = []
    for u in range(n):
        src = None if x_refs is None else x_refs[u]
        first.append(copy(u, 0, me, sibling, src=src))
        first += [copy(u, 1 + j, me, (*chip, c), src=src) for j, chip in enumerate(chips)]
    for cp in first:
        cp.start()
    passed = []
    for j, chip in enumerate(chips):
        for u in range(n):
            copy(u, 1 + j, (*chip, c), me).wait_recv()
            fwd = copy(u, 4 + j, (*chip, c), sibling)
            fwd.start()
            passed.append(fwd)
    for u in range(n):
        copy(u, 0, sibling, me).wait_recv()
    for j, chip in enumerate(chips):
        for u in range(n):
            copy(u, 4 + j, (*chip, 1 - c), me).wait_recv()
    for cp in first + passed:
        cp.wait_send()
    for cp in mines:
        cp.wait()


def _gather_sems(n):
    return [pltpu.SemaphoreType.DMA((n, 7)), pltpu.SemaphoreType.DMA((n, 7)), pltpu.SemaphoreType.DMA((n,))]


def _allgather(blks, *, name, in_vmem):
    n = len(blks)
    space = pltpu.VMEM if in_vmem else pl.ANY

    def body(*refs):
        _gather_blocks(refs[:n], refs[n:2 * n], *refs[2 * n:])

    return pl.pallas_call(
        body, name=name, out_shape=[jax.ShapeDtypeStruct((8,) + b.shape, b.dtype) for b in blks],
        in_specs=[pl.BlockSpec(memory_space=space)] * n, out_specs=[pl.BlockSpec(memory_space=space)] * n,
        scratch_shapes=_gather_sems(n))(*blks)


def _cast_place(ws, c, dev):
    n = len(ws)

    def body(s_ref, *refs):
        for u in range(n):
            refs[n + u][0] = refs[u][...].astype(BF16)

    in_specs, out_specs, out_shape = [], [], []
    for w in ws:
        q, cols = w.shape[0] // 4, w.shape[1]
        in_specs.append(pl.BlockSpec((q, cols), lambda i, s: (2 * s[0] + i, 0)))
        out_specs.append(pl.BlockSpec((1, q, cols), lambda i, s: (s[1], i, 0)))
        out_shape.append(jax.ShapeDtypeStruct((8, 2 * q, cols), BF16))
    return pl.pallas_call(
        body, name="cast_place",
        grid_spec=pltpu.PrefetchScalarGridSpec(num_scalar_prefetch=1, grid=(2,), in_specs=in_specs,
                                               out_specs=out_specs),
        out_shape=_out_hbm(out_shape),
        compiler_params=pltpu.CompilerParams(vmem_limit_bytes=48 << 20))(jnp.stack([c, dev]), *_in_hbm(ws))


def _gather_phases(out_refs, send_sems, recv_sems, rows=None):
    n = len(out_refs)
    x, y, c = _place()
    me, sibling = (x, y, c), (x, y, 1 - c)
    chips = [(1 - x, y), (x, 1 - y), (1 - x, 1 - y)]

    def copy(u, k, block, to):
        px, py, pc = block
        ref = out_refs[u].at[4 * px + 2 * py + pc]
        if rows is not None and rows[u] is not None:
            ref = ref.at[pl.ds(rows[u][0], rows[u][1])]
        return pltpu.make_async_remote_copy(src_ref=ref, dst_ref=ref, send_sem=send_sems.at[u, k],
                                            recv_sem=recv_sems.at[u, k], device_id=to, device_id_type=MESH)

    def start():
        for u in range(n):
            copy(u, 0, me, sibling).start()
            for j, chip in enumerate(chips):
                copy(u, 1 + j, me, (*chip, c)).start()

    def mid():
        for j, chip in enumerate(chips):
            for u in range(n):
                copy(u, 1 + j, (*chip, c), me).wait_recv()
                copy(u, 4 + j, (*chip, c), sibling).start()

    def end():
        for u in range(n):
            copy(u, 0, sibling, me).wait_recv()
        for j, chip in enumerate(chips):
            for u in range(n):
                copy(u, 4 + j, (*chip, 1 - c), me).wait_recv()
        for u in range(n):
            copy(u, 0, me, sibling).wait_send()
            for j, chip in enumerate(chips):
                copy(u, 1 + j, me, (*chip, c)).wait_send()
                copy(u, 4 + j, (*chip, c), sibling).wait_send()

    return start, mid, end


def _carry_gather(bufs, rows=None):
    n = len(bufs)
    return _Carry(bufs, [jax.ShapeDtypeStruct(b.shape, b.dtype) for b in bufs], {u: u for u in range(n)},
                  [pltpu.SemaphoreType.DMA((n, 7)), pltpu.SemaphoreType.DMA((n, 7))],
                  lambda ins, outs, sems: _gather_phases(outs, *sems, rows=rows))


def _allgather_inplace(bufs, *, name):
    n = len(bufs)

    def body(*refs):
        for phase in _gather_phases(refs[n:2 * n], *refs[2 * n:]):
            phase()

    return pl.pallas_call(
        body, name=name, out_shape=[jax.ShapeDtypeStruct(b.shape, b.dtype) for b in bufs],
        in_specs=[ANY] * n, out_specs=[ANY] * n, input_output_aliases={u: u for u in range(n)},
        scratch_shapes=[pltpu.SemaphoreType.DMA((n, 7)), pltpu.SemaphoreType.DMA((n, 7))])(*bufs)


def _ag_small(raw, sq):
    def body(dss1, dg1, dss2, dg2, dnw1, dnw2, dhw4, dqnw8, dknw2, dsk, dlg0, dlg1, sq_ref,
             out_ref, tot_ref, blk, send_sems, recv_sems, local_sems):
        blk[...] = jnp.zeros_like(blk)
        blk[0:2, :] = dss1[1]
        blk[2:3, :] = dg1[...]
        blk[3:5, :] = dss2[...]
        blk[5:6, :] = dg2[...]
        blk[6:8, :] = dss1[0]
        blk[8:9, :] = dnw1[...]
        blk[9:10, :] = dnw2[...]
        blk[10:11, 0:HGW] = dhw4[...]
        blk[10:11, HGW:D] = dqnw8[...]
        blk[11:12, 0:128] = dknw2[...]
        blk[12:14, 0:HGW] = dlg0[0]
        blk[14:16, 0:HGW] = dlg1[0]
        blk[16:24, 0:128] = dsk[...]
        blk[24:25, :] = sq_ref[...]
        _gather_blocks([blk], [out_ref], send_sems, recv_sems, local_sems)
        acc = out_ref[0]
        for i in range(1, 8):
            acc = acc + out_ref[i]
        tot_ref[...] = acc

    vm = pl.BlockSpec(memory_space=pltpu.VMEM)
    return pl.pallas_call(
        body, name="ag_small",
        out_shape=[jax.ShapeDtypeStruct((8, 32, D), F32), jax.ShapeDtypeStruct((32, D), F32)],
        in_specs=[vm] * 13, out_specs=[vm, vm],
        scratch_shapes=[pltpu.VMEM((32, D), F32)] + _gather_sems(1))(*raw, sq)


def _rs_pair_exchange(units):
    n = len(units)

    def body(*refs):
        start, _, end = _pairx_phases(refs[:n], refs[n:2 * n], *refs[2 * n:])
        start()
        end()

    return pl.pallas_call(
        body, name="rs_pair_exchange", out_shape=_pairx_shapes(units),
        in_specs=[ANY] * n, out_specs=[ANY] * n,
        scratch_shapes=[pltpu.SemaphoreType.DMA((n, 4)), pltpu.SemaphoreType.DMA((n, 4))])(*units)


def _pairx_shapes(units):
    return [jax.ShapeDtypeStruct((4,) + g.shape[2:], g.dtype) for g in units]


def _pairx_phases(g_refs, r_refs, send_sems, recv_sems):
    n = len(g_refs)
    x, y, c = _place()
    cps = [pltpu.make_async_remote_copy(
        src_ref=g_refs[u].at[j, 1 - c], dst_ref=r_refs[u].at[j], send_sem=send_sems.at[u, j],
        recv_sem=recv_sems.at[u, j], device_id=(x, y, 1 - c), device_id_type=MESH)
        for u in range(n) for j in range(4)]

    def start():
        for cp in cps:
            cp.start()

    def end():
        for cp in cps:
            cp.wait()

    return start, None, end


def _carry_pairx(units):
    n = len(units)
    return _Carry(units, _pairx_shapes(units), {},
                  [pltpu.SemaphoreType.DMA((n, 4)), pltpu.SemaphoreType.DMA((n, 4))],
                  lambda ins, outs, sems: _pairx_phases(ins, outs, *sems))


def _rs_pair_add(units, recvs, c):
    n = len(units)

    def body(c_ref, *refs):
        for u in range(n):
            refs[2 * n + u][...] = (refs[u][0].astype(F32) + refs[n + u][...].astype(F32)).astype(BF16)

    in_specs, out_specs, out_shape = [], [], []
    for g in units:
        h, w = g.shape[2] // 2, g.shape[3]
        in_specs.append(pl.BlockSpec((1, 1, h, w), lambda j, i, cr: (j, cr[0], i, 0)))
    for g in units:
        h, w = g.shape[2] // 2, g.shape[3]
        in_specs.append(pl.BlockSpec((1, h, w), lambda j, i, cr: (j, i, 0)))
        out_specs.append(pl.BlockSpec((1, h, w), lambda j, i, cr: (j, i, 0)))
        out_shape.append(jax.ShapeDtypeStruct((4, 2 * h, w), BF16))
    return pl.pallas_call(
        body, name="rs_pair_add",
        grid_spec=pltpu.PrefetchScalarGridSpec(num_scalar_prefetch=1, grid=(4, 2), in_specs=in_specs,
                                               out_specs=out_specs),
        out_shape=_out_hbm(out_shape),
        compiler_params=pltpu.CompilerParams(vmem_limit_bytes=48 << 20))(
            c.reshape(1), *_in_hbm(list(units) + list(recvs)))


def _rs_chip_exchange(pairs):
    n = len(pairs)

    def body(*refs):
        start, _, end = _chipx_phases(refs[:n], refs[n:2 * n], *refs[2 * n:])
        start()
        end()

    return pl.pallas_call(
        body, name="rs_chip_exchange", out_shape=[jax.ShapeDtypeStruct(p.shape, p.dtype) for p in pairs],
        in_specs=[ANY] * n, out_specs=[ANY] * n,
        scratch_shapes=[pltpu.SemaphoreType.DMA((n, 3)), pltpu.SemaphoreType.DMA((n, 3))])(*pairs)


def _chipx_phases(p_refs, r_refs, send_sems, recv_sems, rows=None):
    n = len(p_refs)
    x, y, c = _place()
    k = 2 * x + y

    def part(ref):
        return ref if rows is None else ref.at[pl.ds(rows[0], rows[1])]

    sends = []
    for d in range(1, 4):
        j = (k + d) % 4
        for u in range(n):
            sends.append(pltpu.make_async_remote_copy(
                src_ref=part(p_refs[u].at[j]), dst_ref=part(r_refs[u].at[k]), send_sem=send_sems.at[u, d - 1],
                recv_sem=recv_sems.at[u, d - 1], device_id=(j // 2, j % 2, c), device_id_type=MESH))

    def start():
        for cp in sends:
            cp.start()

    def end():
        for d in range(1, 4):
            src = (k + 4 - d) % 4
            for u in range(n):
                pltpu.make_async_remote_copy(
                    src_ref=part(p_refs[u].at[src]), dst_ref=part(r_refs[u].at[src]),
                    send_sem=send_sems.at[u, d - 1], recv_sem=recv_sems.at[u, d - 1], device_id=(x, y, c),
                    device_id_type=MESH).wait_recv()
        for cp in sends:
            cp.wait_send()

    return start, None, end


def _carry_chipx(pairs, rows=None, into=None):
    n = len(pairs)
    sems = [pltpu.SemaphoreType.DMA((n, 3)), pltpu.SemaphoreType.DMA((n, 3))]
    shapes = [jax.ShapeDtypeStruct(p.shape, p.dtype) for p in pairs]
    if into is None:
        return _Carry(pairs, shapes, {}, sems, lambda ins, outs, s: _chipx_phases(ins, outs, *s, rows=rows))
    return _Carry(list(pairs) + list(into), shapes, {n + u: u for u in range(n)}, sems,
                  lambda ins, outs, s: _chipx_phases(ins[:n], outs, *s, rows=rows))


def _rs_chip_add(pairs, contribs, c, chip):
    n = len(pairs)

    def body(s_ref, *refs):
        for u in range(n):
            a, b, c_, d = refs[4 * u:4 * u + 4]
            refs[4 * n + u][0] = ((a[0].astype(F32) + b[0].astype(F32)) + c_[0].astype(F32)) + d[0].astype(F32)

    in_specs, out_specs, out_shape, args = [], [], [], []
    for p, r in zip(pairs, contribs):
        h, w = p.shape[1] // 2, p.shape[2]
        in_specs += [pl.BlockSpec((1, h, w), functools.partial(lambda d, i, s: ((s[1] + d) % 4, i, 0), d))
                     for d in range(4)]
        args += [p, r, r, r]
        out_specs.append(pl.BlockSpec((1, h, w), lambda i, s: (s[0], i, 0)))
        out_shape.append(jax.ShapeDtypeStruct((2, 2 * h, w), F32))
    return pl.pallas_call(
        body, name="rs_chip_add",
        grid_spec=pltpu.PrefetchScalarGridSpec(num_scalar_prefetch=1, grid=(2,), in_specs=in_specs,
                                               out_specs=out_specs),
        out_shape=_out_hbm(out_shape),
        compiler_params=pltpu.CompilerParams(vmem_limit_bytes=48 << 20))(jnp.stack([c, chip]), *_in_hbm(args))


def _rs_sibling_gather(reds):
    n = len(reds)

    def body(*refs):
        start, _, end = _sibx_phases(refs[n:2 * n], *refs[2 * n:])
        start()
        end()

    return pl.pallas_call(
        body, name="rs_sibling_gather", out_shape=[jax.ShapeDtypeStruct(r.shape, r.dtype) for r in reds],
        in_specs=[ANY] * n, out_specs=[ANY] * n, input_output_aliases={u: u for u in range(n)},
        scratch_shapes=[pltpu.SemaphoreType.DMA((n,))] * 2)(*reds)


def _sibx_phases(o_refs, send_sems, recv_sems):
    n = len(o_refs)
    x, y, c = _place()
    cps = [pltpu.make_async_remote_copy(
        src_ref=o_refs[u].at[c], dst_ref=o_refs[u].at[c], send_sem=send_sems.at[u], recv_sem=recv_sems.at[u],
        device_id=(x, y, 1 - c), device_id_type=MESH) for u in range(n)]

    def start():
        for cp in cps:
            cp.start()

    def end():
        for u in range(n):
            cps[u].wait_send()
            pltpu.make_async_remote_copy(
                src_ref=o_refs[u].at[1 - c], dst_ref=o_refs[u].at[1 - c], send_sem=send_sems.at[u],
                recv_sem=recv_sems.at[u], device_id=(x, y, 1 - c), device_id_type=MESH).wait_recv()

    return start, None, end


def _carry_sibx(reds):
    n = len(reds)
    return _Carry(reds, [jax.ShapeDtypeStruct(r.shape, r.dtype) for r in reds], {u: u for u in range(n)},
                  [pltpu.SemaphoreType.DMA((n,))] * 2, lambda ins, outs, sems: _sibx_phases(outs, *sems))


def _prologue(blk, c_ctx, w, b, in8):
    n = w.shape[1]

    def body(blk_ref, cctx_ref, w_ref, b_ref, _in_in, g0_ref, c16_ref, g1_ref, in_ref, mod_s,
             s1, r1, l1, s2, r2, l2, s3, r3):
        start, mid, end = _gather_phases([in_ref], s3, r3)
        start()
        _gather_blocks([blk_ref], [g0_ref], s1, r1, l1)
        c16 = jnp.concatenate([g0_ref[i, 0:1, :] for i in range(8)] + [cctx_ref[...], jnp.zeros((7, D), F32)],
                              axis=0)
        c16_ref[...] = c16
        mod_s[...] = _dot(c16 * _sig(c16), w_ref[...], prec=HI) + b_ref[...]
        _gather_blocks([mod_s], [g1_ref], s2, r2, l2)
        mid()
        end()

    vm = pl.BlockSpec(memory_space=pltpu.VMEM)
    return pl.pallas_call(
        body, name="prologue",
        out_shape=[jax.ShapeDtypeStruct((8, 8, D), F32), jax.ShapeDtypeStruct((16, D), F32),
                   jax.ShapeDtypeStruct((8, 16, n), F32), jax.ShapeDtypeStruct(in8.shape, in8.dtype)],
        in_specs=[vm, vm, vm, vm, ANY], out_specs=[vm, vm, vm, ANY], input_output_aliases={4: 3},
        scratch_shapes=[pltpu.VMEM((16, n), F32)] + _gather_sems(1) + _gather_sems(1)
        + [pltpu.SemaphoreType.DMA((1, 7)), pltpu.SemaphoreType.DMA((1, 7))],
        compiler_params=pltpu.CompilerParams(vmem_limit_bytes=48 << 20))(blk, c_ctx, w, b, in8)


def _ada_bwd(c16, dmod16, w):
    n = w.shape[1]
    tn = 512

    def body(c_ref, d_ref, w_ref, gw_ref, gc_ref):
        j = pl.program_id(0)

        @pl.when(j == 0)
        def _():
            gc_ref[...] = jnp.zeros_like(gc_ref)

        cc = c_ref[...]
        dm = d_ref[...]
        gw_ref[...] = _dot(cc * _sig(cc), dm, TN, prec=HI)
        gc_ref[...] += _dot(dm, w_ref[...], NT, prec=HI)

    return _pcall(body, name="ada_bwd", grid=(n // tn,),
                  in_specs=[_full((16, D)), pl.BlockSpec((16, tn), lambda j: (0, j)),
                            pl.BlockSpec((D, tn), lambda j: (0, j))],
                  out_specs=[pl.BlockSpec((D, tn), lambda j: (0, j)), _full((16, D))],
                  out_shape=[jax.ShapeDtypeStruct((D, n), F32),
                             jax.ShapeDtypeStruct((16, D), F32)])(c16, dmod16, w)


def _adam_math(w, g, m, v):
    c1 = 1.0 - ADAM_B1 ** ADAM_STEP
    c2 = 1.0 - ADAM_B2 ** ADAM_STEP
    nm = ADAM_B1 * m + (1.0 - ADAM_B1) * g
    nv = ADAM_B2 * v + (1.0 - ADAM_B2) * (g * g)
    return -ADAM_LR * ((nm / c1) / (jnp.sqrt(nv / c2) + ADAM_EPS) + ADAM_WD * w), nm, nv


def _adamw_small(ws, gs, ms, vs):
    n = len(ws)

    def body(*refs):
        for u in range(n):
            d_, nm, nv = _adam_math(refs[u][...], refs[n + u][...], refs[2 * n + u][...], refs[3 * n + u][...])
            refs[4 * n + u][...] = d_
            refs[5 * n + u][...] = nm
            refs[6 * n + u][...] = nv

    specs = [_full(w.shape) for w in ws]
    shapes = [jax.ShapeDtypeStruct(w.shape, F32) for w in ws]
    out = _pcall(body, name="adamw_small", grid=(1,), in_specs=specs * 4, out_specs=specs * 3,
                 out_shape=shapes * 3)(*ws, *gs, *ms, *vs)
    return out[:n], out[n:2 * n], out[2 * n:]


def _cctx_grad(parts, c_ctx):
    def body(p_ref, c_ref, o_ref):
        acc = p_ref[0:1, :]
        for k in range(1, 4):
            acc = acc + p_ref[k:k + 1, :]
        cc = c_ref[...]
        s = _sig(cc)
        o_ref[...] = acc * (s * (1.0 + cc * (1.0 - s)))

    return _pcall(body, name="cctx_grad", grid=(1,), in_specs=[_full(parts.shape), _full((1, D))],
                  out_specs=_full((1, D)), out_shape=jax.ShapeDtypeStruct((1, D), F32))(parts, c_ctx)


ADAM_STEPS = 8


def _adamw_multi(ws, gs, ms, vs, *, name, carry=None):
    n = len(ws)

    def body(*refs):
        for u in range(n):
            refs[4 * n + u][...], refs[5 * n + u][...], refs[6 * n + u][...] = _adam_math(
                refs[u][...], refs[n + u][...], refs[2 * n + u][...], refs[3 * n + u][...])

    specs = [pl.BlockSpec((w.shape[0] // ADAM_STEPS, w.shape[1]), lambda i: (i, 0)) for w in ws]
    shapes = [jax.ShapeDtypeStruct(w.shape, F32) for w in ws]
    res = _pcall(body, name=name, grid=(ADAM_STEPS,), in_specs=specs * 4, out_specs=specs * 3,
                 out_shape=shapes * 3, carry=carry)(*ws, *gs, *ms, *vs)
    out, extra = res if carry is not None else (res, None)
    return (out[:n], out[n:2 * n], out[2 * n:]), extra


def kernel(x, c, ctx, c_ctx, w_ada, b_ada, norm_mix_w, norm_ffn_w, w_in, hgrn_lb_logits, hgrn_norm_w, q_norm_w, k_norm_w, attn_sinks, w_branch_hgrn, w_branch_attn, w_out, w_ffn_gate, w_ffn_up, w_ffn_down, loss_target, m_c_ctx, m_w_ada, m_b_ada, m_norm_mix_w, m_norm_ffn_w, m_w_in, m_hgrn_lb_logits, m_hgrn_norm_w, m_q_norm_w, m_k_norm_w, m_attn_sinks, m_w_branch_hgrn, m_w_branch_attn, m_w_out, m_w_ffn_gate, m_w_ffn_up, m_w_ffn_down, v_c_ctx, v_w_ada, v_b_ada, v_norm_mix_w, v_norm_ffn_w, v_w_in, v_hgrn_lb_logits, v_hgrn_norm_w, v_q_norm_w, v_k_norm_w, v_attn_sinks, v_w_branch_hgrn, v_w_branch_attn, v_w_out, v_w_ffn_gate, v_w_ffn_up, v_w_ffn_down):
    xi, yi, ci = _place()
    chip = 2 * xi + yi
    dev = 2 * chip + ci
    s_len = x.shape[1]

    shards = [w_in[0].T, w_branch_hgrn[0], w_branch_attn[0], w_out[0], w_ffn_gate[0].T, w_ffn_up[0].T,
              w_ffn_down[0]]
    bufs = _cast_place(shards, ci, dev)

    lbrow = jnp.pad(hgrn_lb_logits.reshape(1, 512), ((0, 0), (0, D - 512)))
    blk = jnp.concatenate([c, lbrow, jnp.zeros((6, D), F32)], axis=0)
    nada = w_ada.shape[2]
    b_sh = lax.dynamic_slice(b_ada, (0, chip * nada), (1, nada))
    g0, c16, g1, in8 = _prologue(blk, c_ctx[None], w_ada[0], b_sh, bufs[0])
    lg = g0[0::2, 1, :512].reshape(4, 2, 2, 128).transpose(1, 2, 0, 3).reshape(2, 2, HGW)
    modall = g1[0::2].transpose(1, 0, 2).reshape(16, 4 * nada)
    mod = lax.dynamic_slice(modall, (dev, 0), (1, 6 * D)).reshape(6, D)
    modc = modall[8].reshape(6, D)[:2]

    sq, gx, _, small, rs = _local_step(
        x[0], ctx[0], loss_target[0], mod, modc, norm_mix_w, norm_ffn_w, lg, hgrn_norm_w, q_norm_w,
        k_norm_w, attn_sinks[0], in8.reshape(NCOL, D), bufs[1:], dist=(ci, chip))

    def whole(r):
        return r.reshape(2 * r.shape[1], r.shape[2])

    g_dn, g_g, g_u = [whole(r) for r in rs["ffn_done"]]
    g_bh, g_ba, g_o = [whole(r) for r in rs["mix_done"]]
    in_pairs = rs["in_pairs"]
    rest_rows = (in_pairs[0].shape[1] // 2, in_pairs[0].shape[1] // 2)

    g2, tot = _ag_small(small["raw"], sq)
    loss = 0.5 * jnp.sum(tot[24]) / D
    dmodc_tot = jnp.pad(tot[6:8].reshape(1, 2 * D), ((0, 0), (0, 4 * D)))
    g_b_ada = tot[0:6].reshape(1, 6 * D) + dmodc_tot
    dmod16 = jnp.concatenate([g2[:, 0:6].reshape(8, 6 * D), dmodc_tot, jnp.zeros((7, 6 * D), F32)], axis=0)
    g_w_ada, gc_part = _ada_bwd(c16, lax.dynamic_slice(dmod16, (0, chip * nada), (16, nada)), w_ada[0])
    g3, = _allgather([gc_part[8:16]], name="ag_cctx", in_vmem=True)
    g_c_ctx = _cctx_grad(g3[0::2, 0], c_ctx[None])[0]
    g_nw1 = tot[8:9]
    g_nw2 = tot[9:10]
    g_hw = tot[10, :HGW].reshape(4, HGD).sum(0, keepdims=True)
    g_qnw = tot[10, HGW:].reshape(8, HDIM).sum(0, keepdims=True)
    g_knw = tot[11, :128].reshape(2, HDIM).sum(0, keepdims=True)
    g_sinks = tot[16:24, 0][None]
    g_lg = lax.dynamic_slice(tot[12:16, :HGW].reshape(2, 2, HGW), (0, 0, chip * 128), (2, 2, 128))

    names = ["c_ctx", "w_ada", "b_ada", "norm_mix_w", "norm_ffn_w", "w_in", "hgrn_lb_logits", "hgrn_norm_w",
             "q_norm_w", "k_norm_w", "attn_sinks", "w_branch_hgrn", "w_branch_attn", "w_out", "w_ffn_gate",
             "w_ffn_up", "w_ffn_down"]
    ws = dict(zip(names, [c_ctx, w_ada, b_ada, norm_mix_w, norm_ffn_w, w_in, hgrn_lb_logits, hgrn_norm_w,
                          q_norm_w, k_norm_w, attn_sinks, w_branch_hgrn, w_branch_attn, w_out, w_ffn_gate,
                          w_ffn_up, w_ffn_down]))
    ms = dict(zip(names, [m_c_ctx, m_w_ada, m_b_ada, m_norm_mix_w, m_norm_ffn_w, m_w_in, m_hgrn_lb_logits,
                          m_hgrn_norm_w, m_q_norm_w, m_k_norm_w, m_attn_sinks, m_w_branch_hgrn,
                          m_w_branch_attn, m_w_out, m_w_ffn_gate, m_w_ffn_up, m_w_ffn_down]))
    vs = dict(zip(names, [v_c_ctx, v_w_ada, v_b_ada, v_norm_mix_w, v_norm_ffn_w, v_w_in, v_hgrn_lb_logits,
                          v_hgrn_norm_w, v_q_norm_w, v_k_norm_w, v_attn_sinks, v_w_branch_hgrn,
                          v_w_branch_attn, v_w_out, v_w_ffn_gate, v_w_ffn_up, v_w_ffn_down]))
    transposed = ("w_in", "w_ffn_gate", "w_ffn_up")

    def view(a, n):
        return a[0].T if n in transposed else a[0]

    def unview(a, n):
        return a.T[None] if n in transposed else a[None]

    delta, new_m, new_v, grads = {}, {}, {}, {}

    def big_adamw(group, gs, name, carry=None):
        (d_, m_, v_), extra = _adamw_multi([view(ws[n], n) for n in group], gs, [view(ms[n], n) for n in group],
                                           [view(vs[n], n) for n in group], name=name, carry=carry)
        for i, n in enumerate(group):
            grads[n], delta[n], new_m[n], new_v[n] = (unview(gs[i], n), unview(d_[i], n), unview(m_[i], n),
                                                      unview(v_[i], n))
        return extra

    in_contribs = big_adamw(["w_ffn_down", "w_ffn_gate", "w_ffn_up", "w_out", "w_branch_hgrn", "w_branch_attn"],
                            [g_dn, g_g, g_u, g_o, g_bh, g_ba], "adamw_first",
                            carry=_carry_chipx(in_pairs, rows=rest_rows, into=rs["in_part"]))
    in_reds = _rs_chip_add(in_pairs, in_contribs, ci, chip)
    g_in, = [whole(r) for r in _rs_sibling_gather(in_reds)]
    big_adamw(["w_in", "w_ada"], [g_in, g_w_ada], "adamw_second")
    grads.update(c_ctx=g_c_ctx, b_ada=g_b_ada, norm_mix_w=g_nw1, norm_ffn_w=g_nw2, hgrn_lb_logits=g_lg,
                 hgrn_norm_w=g_hw, q_norm_w=g_qnw, k_norm_w=g_knw, attn_sinks=g_sinks)
    small_names = [n for n in names if n not in delta]

    def two_d(a):
        return a.reshape(1, -1) if a.ndim == 1 else a

    sd, sm_, sv = _adamw_small(*[[two_d(d[n]) for n in small_names] for d in (ws, grads, ms, vs)])
    for i, n in enumerate(small_names):
        for dst, src in ((delta, sd), (new_m, sm_), (new_v, sv)):
            dst[n] = src[i].reshape(ws[n].shape)
    return (loss, gx[None], *[grads[n] for n in names], *[delta[n] for n in names],
            *[new_m[n] for n in names], *[new_v[n] for n in names])
```

```python
import functools

import numpy as np
import jax
import jax.numpy as jnp
from jax import lax
from jax.experimental import pallas as pl
from jax.experimental.pallas import tpu as pltpu

F32 = jnp.float32
BF16 = jnp.bfloat16
HI = lax.Precision.HIGHEST
MESH = pl.DeviceIdType.MESH

D = 1024
L = 256
TM = 256
HGW = 512
HGD = 128
CH = 32
ATW = 512
HDIM = 64
BLK = 128
GRID_W = 64
DFF = 2816
NCOL = 5376
EPS = 1e-6
ROPE_THETA = 10000.0

C_FB, C_INP, C_QHG, C_FF = 0, 1, 2, 3
C_GATES = 1
C_GHG, C_QRAW = 8, 9
C_KV = 20
C_QKV = 6

ADAM_LR, ADAM_B1, ADAM_B2, ADAM_EPS, ADAM_WD, ADAM_STEP = 0.001, 0.9, 0.999, 1e-08, 0.01, 10

NN = (((1,), (0,)), ((), ()))
NT = (((1,), (1,)), ((), ()))
TN = (((0,), (0,)), ((), ()))


def _dot(a, b, dims=NN, prec=None):
    return lax.dot_general(a, b, dims, precision=prec, preferred_element_type=F32)


def _bdot(a, b, dims=NN):
    return _dot(a.astype(BF16), b.astype(BF16), dims)


def _sig(x):
    return 1.0 / (1.0 + jnp.exp(-x))


class _Carry:
    def __init__(self, ins, outs, aliases, scratch, phases):
        self.ins, self.outs, self.aliases, self.scratch, self.phases = ins, outs, aliases, scratch, phases


def _in_hbm(args):
    return [pltpu.with_memory_space_constraint(a, pltpu.HBM) for a in args]


def _out_hbm(shapes):
    if isinstance(shapes, (list, tuple)):
        return [pltpu.HBM(s.shape, s.dtype) for s in shapes]
    return pltpu.HBM(shapes.shape, shapes.dtype)


def _carry_join(a, b):
    na_in, na_out, na_sc = len(a.ins), len(a.outs), len(a.scratch)
    aliases = dict(a.aliases)
    aliases.update({na_in + i: na_out + o for i, o in b.aliases.items()})

    def phases(ins, outs, sems):
        pa = a.phases(ins[:na_in], outs[:na_out], sems[:na_sc])
        pb = b.phases(ins[na_in:], outs[na_out:], sems[na_sc:])

        def both(fa, fb):
            if fa is None and fb is None:
                return None

            def run():
                for fn in (fa, fb):
                    if fn is not None:
                        fn()
            return run

        return tuple(both(fa, fb) for fa, fb in zip(pa, pb))

    return _Carry(list(a.ins) + list(b.ins), list(a.outs) + list(b.outs), aliases,
                  list(a.scratch) + list(b.scratch), phases)


def _pcall(body, *, name, grid, in_specs, out_specs, out_shape, scratch=(), aliases=None, vmem_mb=48,
           carry=None):
    params = pltpu.CompilerParams(dimension_semantics=("arbitrary",) * len(grid),
                                  vmem_limit_bytes=vmem_mb << 20)
    if carry is None:
        plain = pl.pallas_call(
            body, name=name, grid=grid, in_specs=in_specs, out_specs=out_specs, out_shape=_out_hbm(out_shape),
            scratch_shapes=list(scratch), input_output_aliases=aliases or {}, compiler_params=params)
        return lambda *args: plain(*_in_hbm(args))
    single = not isinstance(out_shape, (list, tuple))
    out_specs_l = [out_specs] if single else list(out_specs)
    out_shape_l = [out_shape] if single else list(out_shape)
    n_in, n_out, n_sc = len(in_specs), len(out_shape_l), len(scratch)
    k_in, k_out = len(carry.ins), len(carry.outs)
    nsteps = int(np.prod(grid))
    assert nsteps >= 3

    def wrapped(*refs):
        ins, cins = refs[:n_in], refs[n_in:n_in + k_in]
        o0 = n_in + k_in
        outs, couts = refs[o0:o0 + n_out], refs[o0 + n_out:o0 + n_out + k_out]
        s0 = o0 + n_out + k_out
        sc, csc = refs[s0:s0 + n_sc], refs[s0 + n_sc:]
        step = pl.program_id(0)
        for ax in range(1, len(grid)):
            step = step * grid[ax] + pl.program_id(ax)
        start, mid, end = carry.phases(cins, couts, csc)
        pl.when(step == 0)(start)
        body(*ins, *outs, *sc)
        if mid is not None:
            pl.when(step == nsteps - 2)(mid)
        pl.when(step == nsteps - 1)(end)

    all_aliases = dict(aliases or {})
    all_aliases.update({n_in + i: n_out + o for i, o in carry.aliases.items()})
    call = pl.pallas_call(
        wrapped, name=name, grid=grid, in_specs=list(in_specs) + [ANY] * k_in,
        out_specs=out_specs_l + [ANY] * k_out, out_shape=_out_hbm(out_shape_l + list(carry.outs)),
        scratch_shapes=list(scratch) + list(carry.scratch), input_output_aliases=all_aliases,
        compiler_params=params)

    def run(*args):
        res = call(*_in_hbm(args), *carry.ins)
        core = res[:n_out]
        return (core[0] if single else list(core)), list(res[n_out:])

    return run


def _full(shape):
    nd = len(shape)
    return pl.BlockSpec(shape, lambda *_: (0,) * nd)


ANY = pl.BlockSpec(memory_space=pl.ANY)


def _mm(a, b, *, name, mode="nn", out_dtype=F32, tm, tn, tk):
    if mode == "nn":
        (m, k), (k2, n) = a.shape, b.shape
    elif mode == "nt":
        (m, k), (n, k2) = a.shape, b.shape
    else:
        (k, m), (k2, n) = a.shape, b.shape
    assert k == k2 and m % tm == 0 and n % tn == 0 and k % tk == 0, (name, a.shape, b.shape)
    nk = k // tk
    dims = {"nn": NN, "nt": NT, "tn": TN}[mode]

    def body(a_ref, b_ref, o_ref, acc):
        kk = pl.program_id(2)

        @pl.when(kk == 0)
        def _():
            acc[...] = jnp.zeros_like(acc)

        acc[...] += _bdot(a_ref[...], b_ref[...], dims)

        @pl.when(kk == nk - 1)
        def _():
            o_ref[...] = acc[...].astype(out_dtype)

    a_spec = (pl.BlockSpec((tk, tm), lambda i, j, kk: (kk, i)) if mode == "tn"
              else pl.BlockSpec((tm, tk), lambda i, j, kk: (i, kk)))
    b_spec = (pl.BlockSpec((tn, tk), lambda i, j, kk: (j, kk)) if mode == "nt"
              else pl.BlockSpec((tk, tn), lambda i, j, kk: (kk, j)))
    return _pcall(body, name=name, grid=(m // tm, n // tn, nk), in_specs=[a_spec, b_spec],
                  out_specs=pl.BlockSpec((tm, tn), lambda i, j, kk: (i, j)),
                  out_shape=jax.ShapeDtypeStruct((m, n), out_dtype),
                  scratch=[pltpu.VMEM((tm, tn), F32)])(a, b)


NT_IN = NCOL // 256


def _src_block(j):
    return j + jnp.where(j < 4, 2, jnp.where(j < 6, 3, jnp.where(j < 8, -6, jnp.where(
        j < 16, 5, jnp.where(j < 20, -7, -14)))))


def _mm_in(h, wt, tm, carry=None):
    tt = h.shape[0]

    def body(h_ref, w_ref, o_ref):
        o_ref[...] = _bdot(h_ref[...], w_ref[...], NT)

    return _pcall(body, name="mm_in", grid=(tt // tm, NT_IN),
                  in_specs=[pl.BlockSpec((tm, D), lambda i, j: (i, 0)),
                            pl.BlockSpec((256, D), lambda i, j: (_src_block(j), 0))],
                  out_specs=pl.BlockSpec((tm, 256), lambda i, j: (i, j)),
                  out_shape=jax.ShapeDtypeStruct((tt, NCOL), F32), carry=carry)(h, wt)


def _mm_dh(dp, wt, tm, carry=None):
    tt = dp.shape[0]
    per, ng = 3, NT_IN // 3

    def body(d_ref, w0, w1, w2, o_ref, acc):
        kk = pl.program_id(1)

        @pl.when(kk == 0)
        def _():
            acc[...] = jnp.zeros_like(acc)

        acc[...] += (_bdot(d_ref[:, 0:256], w0[...]) + _bdot(d_ref[:, 256:512], w1[...])
                     + _bdot(d_ref[:, 512:768], w2[...]))

        @pl.when(kk == ng - 1)
        def _():
            o_ref[...] = acc[...]

    wspecs = [pl.BlockSpec((256, D), functools.partial(lambda t, i, kk: (_src_block(per * kk + t), 0), t))
              for t in range(per)]
    return _pcall(body, name="mm_dh", grid=(tt // tm, ng),
                  in_specs=[pl.BlockSpec((tm, per * 256), lambda i, kk: (i, kk))] + wspecs,
                  out_specs=pl.BlockSpec((tm, D), lambda i, kk: (i, 0)),
                  out_shape=jax.ShapeDtypeStruct((tt, D), F32), scratch=[pltpu.VMEM((tm, D), F32)],
                  carry=carry)(dp, wt, wt, wt)


def _mm_gin(dp, h, tk):
    tt = dp.shape[0]
    nk = tt // tk

    def body(d_ref, h_ref, o_ref, acc):
        kk = pl.program_id(1)

        @pl.when(kk == 0)
        def _():
            acc[...] = jnp.zeros_like(acc)

        acc[...] += _bdot(d_ref[...], h_ref[...], TN)

        @pl.when(kk == nk - 1)
        def _():
            o_ref[...] = acc[...].astype(BF16)

    return _pcall(body, name="mm_gin", grid=(NT_IN, nk),
                  in_specs=[pl.BlockSpec((tk, 256), lambda j, kk: (kk, j)),
                            pl.BlockSpec((tk, D), lambda j, kk: (kk, 0))],
                  out_specs=pl.BlockSpec((256, D), lambda j, kk: (_src_block(j), 0)),
                  out_shape=jax.ShapeDtypeStruct((NCOL, D), BF16), scratch=[pltpu.VMEM((256, D), F32)])(dp, h)


def _tok_specs():
    assert L == TM
    return [_full((TM, D)), pl.BlockSpec((TM, D), lambda i: (jnp.maximum(i - 1, 0), 0))]


def _mod1(ctx, x, nw, ss):
    rows = L + x.shape[0]

    def body(c_ref, x_ref, nw_ref, ss_ref, h_ref):
        t = jnp.where(pl.program_id(0) == 0, c_ref[...], x_ref[...])
        r = lax.rsqrt(jnp.mean(t * t, axis=-1, keepdims=True) + EPS)
        s = ss_ref[0]
        h_ref[...] = ((t * r * nw_ref[...]) * (1.0 + s[1:2]) + s[0:1]).astype(BF16)

    return _pcall(body, name="mod1", grid=(rows // TM,),
                  in_specs=_tok_specs() + [_full((1, D)),
                                           pl.BlockSpec((1, 2, D), lambda i: (jnp.minimum(i, 1), 0, 0))],
                  out_specs=pl.BlockSpec((TM, D), lambda i: (i, 0)),
                  out_shape=jax.ShapeDtypeStruct((rows, D), BF16))(ctx, x, nw, ss)


def _norm_bwd_rows(x, dh, nw, scale):
    r = lax.rsqrt(jnp.mean(x * x, axis=-1, keepdims=True) + EPS)
    xh = x * r
    dxh = dh * ((1.0 + scale) * nw)
    dx = r * (dxh - xh * jnp.mean(dxh * xh, axis=-1, keepdims=True))
    return dx, xh


def _out_proj_mod2(mixed, w_o, x, g1, nw2, ss2):
    s_len = x.shape[0]
    tm = 512

    def body(m_ref, w_ref, x_ref, g_ref, nw_ref, ss_ref, ao_ref, x1_ref, h_ref):
        ao = _bdot(m_ref[...], w_ref[...])
        ao_ref[...] = ao
        x1 = x_ref[...] + g_ref[...] * ao
        x1_ref[...] = x1
        r = lax.rsqrt(jnp.mean(x1 * x1, axis=-1, keepdims=True) + EPS)
        s = ss_ref[0]
        h_ref[...] = ((x1 * r * nw_ref[...]) * (1.0 + s[1:2]) + s[0:1]).astype(BF16)

    row = pl.BlockSpec((tm, D), lambda i: (i, 0))
    f = jax.ShapeDtypeStruct((s_len, D), F32)
    return _pcall(body, name="out_proj_mod2", grid=(s_len // tm,),
                  in_specs=[row, _full((D, D)), row, _full((1, D)), _full((1, D)), _full((1, 2, D))],
                  out_specs=[row, row, row],
                  out_shape=[f, f, jax.ShapeDtypeStruct((s_len, D), BF16)])(mixed, w_o, x, g1, nw2, ss2)


TS = 1024


def _acc_call(body, *, name, grid, in_specs, out_specs, out_shape, acc_shapes, args, carry=None):
    return _pcall(body, name=name, grid=grid, in_specs=in_specs, out_specs=out_specs, out_shape=out_shape,
                  scratch=[pltpu.VMEM(s, F32) for s in acc_shapes], carry=carry)(*args)


def _mm_cs(a, w4, *, name):
    m, k = a.shape
    _, _, ns = w4.shape

    def body(a_ref, w_ref, o_ref):
        o_ref[...] = _bdot(a_ref[...], w_ref[0])

    return _pcall(body, name=name, grid=(m // TS, 4),
                  in_specs=[pl.BlockSpec((TS, k), lambda i, j: (i, 0)),
                            pl.BlockSpec((1, k, ns), lambda i, j: (j, 0, 0))],
                  out_specs=pl.BlockSpec((TS, ns), lambda i, j: (i, j)),
                  out_shape=jax.ShapeDtypeStruct((m, 4 * ns), F32))(a, w4)


def _mm_cs_nt(a, w4, *, name):
    m = a.shape[0]
    _, k, ns = w4.shape

    def body(a_ref, w_ref, o_ref, acc):
        j = pl.program_id(1)

        @pl.when(j == 0)
        def _():
            acc[...] = jnp.zeros_like(acc)

        acc[...] += _bdot(a_ref[...], w_ref[0], NT)

        @pl.when(j == 3)
        def _():
            o_ref[...] = acc[...]

    return _acc_call(body, name=name, grid=(m // TS, 4),
                     in_specs=[pl.BlockSpec((TS, ns), lambda i, j: (i, j)),
                               pl.BlockSpec((1, k, ns), lambda i, j: (j, 0, 0))],
                     out_specs=pl.BlockSpec((TS, k), lambda i, j: (i, 0)),
                     out_shape=jax.ShapeDtypeStruct((m, k), F32), acc_shapes=[(TS, k)], args=(a, w4))


def _mm_cs_tn(a, b, ns, *, name):
    s_len, k = a.shape
    nk = s_len // TS

    def body(a_ref, b_ref, o_ref, acc):
        t = pl.program_id(1)

        @pl.when(t == 0)
        def _():
            acc[...] = jnp.zeros_like(acc)

        acc[...] += _bdot(a_ref[...], b_ref[...], TN)

        @pl.when(t == nk - 1)
        def _():
            o_ref[0] = acc[...].astype(o_ref.dtype)

    return _acc_call(body, name=name, grid=(4, nk),
                     in_specs=[pl.BlockSpec((TS, k), lambda j, t: (t, 0)),
                               pl.BlockSpec((TS, ns), lambda j, t: (t, j))],
                     out_specs=pl.BlockSpec((1, k, ns), lambda j, t: (j, 0, 0)),
                     out_shape=jax.ShapeDtypeStruct((4, k, ns), BF16), acc_shapes=[(k, ns)], args=(a, b))


def _ffn_up(h2, g4, u4, carry=None):
    s_len = h2.shape[0]
    ns = g4.shape[1]

    def body(h_ref, g_ref, u_ref, a_ref, b_ref, z_ref):
        h = h_ref[...]
        a = _bdot(h, g_ref[0], NT)
        b = _bdot(h, u_ref[0], NT)
        a_ref[0] = a.astype(BF16)
        b_ref[0] = b.astype(BF16)
        z_ref[0] = (a * _sig(a) * b).astype(BF16)

    w = pl.BlockSpec((1, ns, D), lambda i, j: (j, 0, 0))
    o = pl.BlockSpec((1, TS, ns), lambda i, j: (j, i, 0))
    f = jax.ShapeDtypeStruct((4, s_len, ns), BF16)
    return _pcall(body, name="ffn_up", grid=(s_len // TS, 4),
                  in_specs=[pl.BlockSpec((TS, D), lambda i, j: (i, 0)), w, w], out_specs=[o, o, o],
                  out_shape=[f, f, jax.ShapeDtypeStruct((4, s_len, ns), BF16)], carry=carry)(h2, g4, u4)


def _ffn_down_loss(z4, dn4, x1, g2, tgt):
    _, s_len, ns = z4.shape

    def body(z_ref, w_ref, x1_ref, g_ref, t_ref, sq_ref, dx2_ref, dyb_ref, dg_ref, acc):
        i, j = pl.program_id(0), pl.program_id(1)

        @pl.when((i == 0) & (j == 0))
        def _():
            sq_ref[...] = jnp.zeros_like(sq_ref)
            dg_ref[...] = jnp.zeros_like(dg_ref)

        @pl.when(j == 0)
        def _():
            acc[...] = jnp.zeros_like(acc)

        acc[...] += _bdot(z_ref[0], w_ref[0])

        @pl.when(j == 3)
        def _():
            y_ = acc[...]
            g = g_ref[...]
            e = x1_ref[...] + g * y_ - t_ref[...]
            sq_ref[...] += jnp.sum(e * e, axis=0, keepdims=True)
            dx2 = e * (1.0 / D)
            dx2_ref[...] = dx2
            dyb_ref[...] = (g * dx2).astype(BF16)
            dg_ref[...] += jnp.sum(dx2 * y_, axis=0, keepdims=True)

    row = pl.BlockSpec((TS, D), lambda i, j: (i, 0))
    vec = _full((1, D))
    return _acc_call(body, name="ffn_down_loss", grid=(s_len // TS, 4),
                     in_specs=[pl.BlockSpec((1, TS, ns), lambda i, j: (j, i, 0)),
                               pl.BlockSpec((1, ns, D), lambda i, j: (j, 0, 0)), row, vec, row],
                     out_specs=[vec, row, row, vec],
                     out_shape=[jax.ShapeDtypeStruct((1, D), F32), jax.ShapeDtypeStruct((s_len, D), F32),
                                jax.ShapeDtypeStruct((s_len, D), BF16), jax.ShapeDtypeStruct((1, D), F32)],
                     acc_shapes=[(TS, D)], args=(z4, dn4, x1, g2, tgt))


def _ffn_dz(dyb, dn4, a4, b4):
    _, s_len, ns = a4.shape

    def body(dy_ref, w_ref, a_ref, b_ref, da_ref, db_ref):
        dz = _bdot(dy_ref[...], w_ref[0], NT)
        a = a_ref[0].astype(F32)
        s = _sig(a)
        da_ref[0] = (dz * b_ref[0].astype(F32) * (s * (1.0 + a * (1.0 - s)))).astype(BF16)
        db_ref[0] = (dz * (a * s)).astype(BF16)

    t = pl.BlockSpec((1, TS, ns), lambda i, j: (j, i, 0))
    o = jax.ShapeDtypeStruct((4, s_len, ns), BF16)
    return _pcall(body, name="ffn_dz", grid=(s_len // TS, 4),
                  in_specs=[pl.BlockSpec((TS, D), lambda i, j: (i, 0)),
                            pl.BlockSpec((1, ns, D), lambda i, j: (j, 0, 0)), t, t],
                  out_specs=[t, t], out_shape=[o, o])(dyb, dn4, a4, b4)


def _ffn_gdn(z4, dyb):
    _, s_len, ns = z4.shape
    nk = s_len // TS

    def body(z_ref, dy_ref, o_ref, acc):
        t = pl.program_id(1)

        @pl.when(t == 0)
        def _():
            acc[...] = jnp.zeros_like(acc)

        acc[...] += _bdot(z_ref[0], dy_ref[...], TN)

        @pl.when(t == nk - 1)
        def _():
            o_ref[0] = acc[...].astype(o_ref.dtype)

    return _acc_call(body, name="ffn_gdn", grid=(4, nk),
                     in_specs=[pl.BlockSpec((1, TS, ns), lambda j, t: (j, t, 0)),
                               pl.BlockSpec((TS, D), lambda j, t: (t, 0))],
                     out_specs=pl.BlockSpec((1, ns, D), lambda j, t: (j, 0, 0)),
                     out_shape=jax.ShapeDtypeStruct((4, ns, D), BF16), acc_shapes=[(ns, D)], args=(z4, dyb))


def _ffn_dh2(da4, db4, g4, u4, carry=None):
    _, s_len, ns = da4.shape

    def body(da_ref, db_ref, g_ref, u_ref, o_ref, acc):
        j = pl.program_id(1)

        @pl.when(j == 0)
        def _():
            acc[...] = jnp.zeros_like(acc)

        acc[...] += _bdot(da_ref[0], g_ref[0]) + _bdot(db_ref[0], u_ref[0])

        @pl.when(j == 3)
        def _():
            o_ref[...] = acc[...]

    t = pl.BlockSpec((1, TS, ns), lambda i, j: (j, i, 0))
    w = pl.BlockSpec((1, ns, D), lambda i, j: (j, 0, 0))
    return _acc_call(body, name="ffn_dh2", grid=(s_len // TS, 4), in_specs=[t, t, w, w],
                     out_specs=pl.BlockSpec((TS, D), lambda i, j: (i, 0)),
                     out_shape=jax.ShapeDtypeStruct((s_len, D), F32), acc_shapes=[(TS, D)],
                     args=(da4, db4, g4, u4), carry=carry)


def _ffn_ggu(h2, da4, db4, carry=None):
    _, s_len, ns = da4.shape
    nk = s_len // TS

    def body(h_ref, da_ref, db_ref, gg_ref, gu_ref, acc_g, acc_u):
        t = pl.program_id(1)

        @pl.when(t == 0)
        def _():
            acc_g[...] = jnp.zeros_like(acc_g)
            acc_u[...] = jnp.zeros_like(acc_u)

        h = h_ref[...]
        acc_g[...] += _bdot(da_ref[0], h, TN)
        acc_u[...] += _bdot(db_ref[0], h, TN)

        @pl.when(t == nk - 1)
        def _():
            gg_ref[0] = acc_g[...].astype(BF16)
            gu_ref[0] = acc_u[...].astype(BF16)

    d = pl.BlockSpec((1, TS, ns), lambda j, t: (j, t, 0))
    o = pl.BlockSpec((1, ns, D), lambda j, t: (j, 0, 0))
    f = jax.ShapeDtypeStruct((4, ns, D), BF16)
    return _acc_call(body, name="ffn_ggu", grid=(4, nk),
                     in_specs=[pl.BlockSpec((TS, D), lambda j, t: (t, 0)), d, d], out_specs=[o, o],
                     out_shape=[f, f], acc_shapes=[(ns, D), (ns, D)], args=(h2, da4, db4), carry=carry)


def _mod2_bwd(x1, dh2, dx2, ao, nw2, ss2, g1):
    s_len = x1.shape[0]

    def body(x1_ref, dh_ref, dx2_ref, ao_ref, nw_ref, ss_ref, g_ref,
             dx1_ref, da_ref, dss_ref, dnw_ref, dg_ref):
        i = pl.program_id(0)

        @pl.when(i == 0)
        def _():
            dss_ref[...] = jnp.zeros_like(dss_ref)
            dnw_ref[...] = jnp.zeros_like(dnw_ref)
            dg_ref[...] = jnp.zeros_like(dg_ref)

        dh = dh_ref[...]
        nw = nw_ref[...]
        scale = ss_ref[0][1:2]
        dxn, xh = _norm_bwd_rows(x1_ref[...], dh, nw, scale)
        dx1 = dx2_ref[...] + dxn
        dx1_ref[...] = dx1
        da_ref[...] = (g_ref[...] * dx1).astype(BF16)
        dg_ref[...] += jnp.sum(dx1 * ao_ref[...], axis=0, keepdims=True)
        dsh = jnp.sum(dh, axis=0, keepdims=True)
        dsc = jnp.sum(dh * xh * nw, axis=0, keepdims=True)
        dss_ref[...] += jnp.concatenate([dsh, dsc], axis=0)
        dnw_ref[...] += jnp.sum(dh * xh * (1.0 + scale), axis=0, keepdims=True)

    row = pl.BlockSpec((TM, D), lambda i: (i, 0))
    vec = _full((1, D))
    return _pcall(body, name="mod2_bwd", grid=(s_len // TM,),
                  in_specs=[row, row, row, row, vec, _full((1, 2, D)), vec],
                  out_specs=[row, row, _full((2, D)), vec, vec],
                  out_shape=[jax.ShapeDtypeStruct((s_len, D), F32), jax.ShapeDtypeStruct((s_len, D), BF16),
                             jax.ShapeDtypeStruct((2, D), F32), jax.ShapeDtypeStruct((1, D), F32),
                             jax.ShapeDtypeStruct((1, D), F32)])(x1, dh2, dx2, ao, nw2, ss2, g1)


def _mod1_bwd(ctx, x, dh, dx1, nw1, ss1, carry=None):
    s_len = dx1.shape[0]
    tt = L + s_len

    def body(c_ref, x_ref, dh_ref, dx1_ref, nw_ref, ss_ref, dx_ref, dss_ref, dnw_ref):
        i = pl.program_id(0)
        tok = jnp.where(i == 0, c_ref[...], x_ref[...])

        @pl.when(i == 0)
        def _():
            dnw_ref[...] = jnp.zeros_like(dnw_ref)

        @pl.when(i <= 1)
        def _():
            dss_ref[...] = jnp.zeros_like(dss_ref)

        dh_ = dh_ref[...]
        nw = nw_ref[...]
        scale = ss_ref[0][1:2]
        dxn, xh = _norm_bwd_rows(tok, dh_, nw, scale)

        @pl.when(i >= 1)
        def _():
            dx_ref[...] = dx1_ref[...] + dxn

        dsh = jnp.sum(dh_, axis=0, keepdims=True)
        dsc = jnp.sum(dh_ * xh * nw, axis=0, keepdims=True)
        dss_ref[...] += jnp.concatenate([dsh, dsc], axis=0)[None]
        dnw_ref[...] += jnp.sum(dh_ * xh * (1.0 + scale), axis=0, keepdims=True)

    row = pl.BlockSpec((TM, D), lambda i: (i, 0))
    lat = pl.BlockSpec((TM, D), lambda i: (jnp.maximum(i - 1, 0), 0))
    sel = pl.BlockSpec((1, 2, D), lambda i: (jnp.minimum(i, 1), 0, 0))
    return _pcall(body, name="mod1_bwd", grid=(tt // TM,),
                  in_specs=_tok_specs() + [row, lat, _full((1, D)), sel],
                  out_specs=[lat, sel, _full((1, D))],
                  out_shape=[jax.ShapeDtypeStruct((s_len, D), F32), jax.ShapeDtypeStruct((2, 2, D), F32),
                             jax.ShapeDtypeStruct((1, D), F32)], carry=carry)(ctx, x, dh, dx1, nw1, ss1)


def _rows(c):
    return slice(c * CH, (c + 1) * CH)


def _chunk_masks(rev, transpose=False):
    r = lax.broadcasted_iota(jnp.int32, (TM, TM), 0)
    c = lax.broadcasted_iota(jnp.int32, (TM, TM), 1)
    same = (r // CH) == (c // CH)
    before = (c >= r) if (rev != transpose) else (c <= r)
    return same & before, same


def _chunk_scan(x, rev, transpose=False):
    r = lax.broadcasted_iota(jnp.int32, (CH, CH), 0)
    c = lax.broadcasted_iota(jnp.int32, (CH, CH), 1)
    tri = ((c >= r) if (rev != transpose) else (c <= r)).astype(F32)
    return jnp.concatenate([_dot(tri, x[_rows(ch)], prec=HI) for ch in range(x.shape[0] // CH)], axis=0)


def _chunk_total(x):
    return jnp.concatenate([jnp.broadcast_to(jnp.sum(x[_rows(ch)], axis=0, keepdims=True), (CH, x.shape[1]))
                            for ch in range(x.shape[0] // CH)], axis=0)


def _hgrn_gate(fl, qraw, lg):
    lb = 1.0 / (1.0 + jnp.exp(lg[1:2] - lg[0:1]))
    sg = _sig(fl)
    f = lb + (1.0 - lb) * sg
    q = qraw * _sig(qraw) * (HGD ** -0.5)
    return lb, sg, f, q


def _hgrn_fwd(p, lg, *, rev, carry=None):
    tt = p.shape[0]
    nt = tt // TM
    ncht = TM // CH
    d = 1 if rev else 0

    def tile_of(s):
        return jnp.where(s == 0, 0, nt - s) if rev else s

    def body(f_ref, inp_ref, q_ref, lg_ref, o_ref, st_ref, state):
        s = pl.program_id(0)

        @pl.when(s == 0)
        def _():
            state[...] = jnp.zeros_like(state)

        _, _, f, q = _hgrn_gate(f_ref[...], q_ref[...], lg_ref[0])
        lf = jnp.log(f)
        causal, _ = _chunk_masks(rev)
        cum = _chunk_scan(lf, rev)
        tot = _chunk_total(lf)
        qd = (q * jnp.exp(cum)).astype(BF16)
        kd = ((1.0 - f) * jnp.exp(-cum)).astype(BF16)
        ke = ((1.0 - f) * jnp.exp(tot - cum)).astype(BF16)
        et = jnp.exp(tot)
        v = inp_ref[...].astype(BF16)
        order = range(ncht - 1, -1, -1) if rev else range(ncht)
        outs = []
        for h in range(4):
            sl = slice(h * HGD, (h + 1) * HGD)
            qd_, kd_, ke_, v_ = qd[:, sl], kd[:, sl], ke[:, sl], v[:, sl]
            pm = jnp.where(causal, _dot(qd_, kd_, NT), 0.0).astype(BF16)
            o_h = _dot(pm, v_)
            upd = [_dot(v_[_rows(c)], ke_[_rows(c)], TN) for c in range(ncht)]
            st = state[h]
            for c in order:
                st_ref[c, h] = st
                st = st * et[c * CH:c * CH + 1, sl] + upd[c]
            state[h] = st
            inter = [_dot(qd_[_rows(c)], st_ref[c, h].astype(BF16), NT) for c in range(ncht)]
            outs.append(o_h + jnp.concatenate(inter, axis=0))
        o_ref[...] = jnp.concatenate(outs, axis=1)

    def col(cb):
        return pl.BlockSpec((TM, HGW), lambda s: (tile_of(s), cb))

    return _pcall(
        body, name="hgrn_fwd_rev" if rev else "hgrn_fwd", grid=(nt,),
        in_specs=[col(C_FB if rev else C_FF), col(C_INP), col(C_QHG),
                  pl.BlockSpec((1, 2, HGW), lambda s: (d, 0, 0))],
        out_specs=[pl.BlockSpec((TM, HGW), lambda s: (tile_of(s), 0)),
                   pl.BlockSpec((ncht, 4, HGD, HGD), lambda s: (tile_of(s), 0, 0, 0))],
        out_shape=[jax.ShapeDtypeStruct((tt, HGW), F32),
                   jax.ShapeDtypeStruct((nt * ncht, 4, HGD, HGD), F32)],
        scratch=[pltpu.VMEM((4, HGD, HGD), F32)], carry=carry)(p, p, p, lg)


def _hgrn_bwd(p, lg, do, st, dp, prev, *, rev, carry=None):
    tt = p.shape[0]
    nt = tt // TM
    ncht = TM // CH
    d = 1 if rev else 0
    second = prev is not None

    def tile_of(s):
        return jnp.where(s == nt - 1, 0, s + 1) if rev else nt - 1 - s

    def body(*refs):
        if second:
            (f_ref, inp_ref, q_ref, lg_ref, do_ref, st_ref, dvp_ref, dqp_ref, _dp_in,
             dp_ref, dlg_ref, dstate) = refs
        else:
            (f_ref, inp_ref, q_ref, lg_ref, do_ref, st_ref, _dp_in,
             dp_ref, dv_ref, dq_ref, dlg_ref, dstate) = refs
        s = pl.program_id(0)
        tile = tile_of(s)

        @pl.when(s == 0)
        def _():
            dstate[...] = jnp.zeros_like(dstate)
            dlg_ref[...] = jnp.zeros_like(dlg_ref)

        qraw = q_ref[...]
        lb, sg, f, q = _hgrn_gate(f_ref[...], qraw, lg_ref[0])
        lf = jnp.log(f)
        causal, _ = _chunk_masks(rev)
        causal_t, _ = _chunk_masks(rev, transpose=True)
        cum = _chunk_scan(lf, rev)
        tot = _chunk_total(lf)
        ea, eb, ee, et = jnp.exp(cum), jnp.exp(-cum), jnp.exp(tot - cum), jnp.exp(tot)
        qdf, kdf, kef = q * ea, (1.0 - f) * eb, (1.0 - f) * ee
        qd, kd, ke = qdf.astype(BF16), kdf.astype(BF16), kef.astype(BF16)
        v = inp_ref[...].astype(BF16)
        dob = jnp.where(tile == 0, 0.0, do_ref[...]).astype(BF16)
        order = range(ncht) if rev else range(ncht - 1, -1, -1)
        dq_l, dk_l, dv_l, dcum_l, dtot_l = [], [], [], [], []
        for h in range(4):
            sl = slice(h * HGD, (h + 1) * HGD)
            qd_, kd_, ke_, v_, do_ = qd[:, sl], kd[:, sl], ke[:, sl], v[:, sl], dob[:, sl]
            pmt = jnp.where(causal_t, _dot(kd_, qd_, NT), 0.0).astype(BF16)
            dpm = jnp.where(causal, _dot(do_, v_, NT), 0.0).astype(BF16)
            dpmt = jnp.where(causal_t, _dot(v_, do_, NT), 0.0).astype(BF16)
            dv = _dot(pmt, do_)
            dqd = _dot(dpm, kd_)
            dkd = _dot(dpmt, qd_)
            upd = [_dot(do_[_rows(c)], qd_[_rows(c)], TN) for c in range(ncht)]
            ds = dstate[h]
            ds1 = [None] * ncht
            for c in order:
                ds1[c] = ds
                ds = ds * et[c * CH:c * CH + 1, sl] + upd[c]
            dstate[h] = ds
            dke_c, dv_c, dqd_c, dtot_c = [], [], [], []
            for c in range(ncht):
                st0 = st_ref[c, h]
                dsb = ds1[c].astype(BF16)
                dke_ = _dot(v_[_rows(c)], dsb)
                dke_c.append(dke_)
                dv_c.append(_dot(ke_[_rows(c)], dsb, NT))
                dqd_c.append(_dot(do_[_rows(c)], st0.astype(BF16)))
                dt = (jnp.sum(ds1[c] * st0, axis=0, keepdims=True) * et[c * CH:c * CH + 1, sl]
                      + jnp.sum(dke_ * kef[_rows(c), sl], axis=0, keepdims=True))
                dtot_c.append(jnp.broadcast_to(dt, (CH, HGD)))
            dke = jnp.concatenate(dke_c, axis=0)
            dqd = dqd + jnp.concatenate(dqd_c, axis=0)
            dv_l.append(dv + jnp.concatenate(dv_c, axis=0))
            dtot_l.append(jnp.concatenate(dtot_c, axis=0))
            dq_l.append(dqd * ea[:, sl])
            dk_l.append(dkd * eb[:, sl] + dke * ee[:, sl])
            dcum_l.append(dqd * qdf[:, sl] - dkd * kdf[:, sl] - dke * kef[:, sl])
        dcum = jnp.concatenate(dcum_l, axis=1)
        dlf = _chunk_scan(dcum, rev, transpose=True) + jnp.concatenate(dtot_l, axis=1)
        dq_t = jnp.concatenate(dq_l, axis=1)
        dv_t = jnp.concatenate(dv_l, axis=1)

        df = dlf / f - jnp.concatenate(dk_l, axis=1)
        dfl = df * (1.0 - lb) * sg * (1.0 - sg)
        dlb = jnp.sum(df * (1.0 - sg), axis=0, keepdims=True)
        dl0 = dlb * lb * (1.0 - lb)
        dlg_ref[...] += jnp.concatenate([dl0, -dl0], axis=0)[None]
        if second:
            sq = _sig(qraw)
            dqr = (dqp_ref[...] + dq_t) * (HGD ** -0.5) * (sq * (1.0 + qraw * (1.0 - sq)))
            dp_ref[...] = jnp.concatenate([dfl, dvp_ref[...] + dv_t, dqr], axis=1).astype(BF16)
        else:
            dp_ref[...] = dfl.astype(BF16)
            dv_ref[...] = dv_t
            dq_ref[...] = dq_t

    def col(cb):
        return pl.BlockSpec((TM, HGW), lambda s: (tile_of(s), cb))

    tok = pl.BlockSpec((TM, HGW), lambda s: (tile_of(s), 0))
    in_specs = [col(C_FB if rev else C_FF), col(C_INP), col(C_QHG),
                pl.BlockSpec((1, 2, HGW), lambda s: (d, 0, 0)),
                pl.BlockSpec((TM, HGW), lambda s: (jnp.maximum(tile_of(s) - 1, 0), 0)),
                pl.BlockSpec((ncht, 4, HGD, HGD), lambda s: (tile_of(s), 0, 0, 0))]
    args = [p, p, p, lg, do, st]
    dlg_spec = _full((1, 2, HGW))
    dlg_shape = jax.ShapeDtypeStruct((1, 2, HGW), F32)
    if second:
        in_specs += [tok, tok]
        args += [prev[0], prev[1]]
        out_specs = [pl.BlockSpec((TM, 3 * HGW), lambda s: (tile_of(s), 0)), dlg_spec]
        out_shape = [jax.ShapeDtypeStruct(dp.shape, BF16), dlg_shape]
    else:
        out_specs = [pl.BlockSpec((TM, HGW), lambda s: (tile_of(s), C_FB if rev else C_FF)), tok, tok, dlg_spec]
        out_shape = [jax.ShapeDtypeStruct(dp.shape, BF16), jax.ShapeDtypeStruct((tt, HGW), F32),
                     jax.ShapeDtypeStruct((tt, HGW), F32), dlg_shape]
    in_specs.append(ANY)
    args.append(dp)
    return _pcall(body, name="hgrn_bwd_rev" if rev else "hgrn_bwd", grid=(nt,),
                  in_specs=in_specs, out_specs=out_specs, out_shape=out_shape,
                  scratch=[pltpu.VMEM((4, HGD, HGD), F32)],
                  aliases={len(args) - 1: 0}, carry=carry)(*args)


def _head_rms(o, w, nheads):
    outs = []
    for h in range(nheads):
        oh = o[:, h * HGD:(h + 1) * HGD]
        outs.append(oh * lax.rsqrt(jnp.mean(oh * oh, axis=-1, keepdims=True) + EPS))
    return jnp.concatenate(outs, axis=1)


def _readout(o0, o1, p, hw4):
    s_len = o0.shape[0] - L

    def body(o0_ref, o1_ref, g_ref, w_ref, y_ref):
        xh = _head_rms(o0_ref[...] + o1_ref[...], None, 4)
        g = g_ref[...]
        y_ref[...] = (xh * w_ref[...] * (g * _sig(g))).astype(BF16)

    lat = pl.BlockSpec((TM, HGW), lambda i: (i + 1, 0))
    return _pcall(body, name="readout", grid=(s_len // TM,),
                  in_specs=[lat, lat, pl.BlockSpec((TM, HGW), lambda i: (i + 1, C_GHG)), _full((1, HGW))],
                  out_specs=pl.BlockSpec((TM, HGW), lambda i: (i, 0)),
                  out_shape=jax.ShapeDtypeStruct((s_len, HGW), BF16))(o0, o1, p, hw4)


def _readout_bwd(o0, o1, p, hw4, dy, dp, carry=None):
    tt = o0.shape[0]
    s_len = tt - L

    def body(o0_ref, o1_ref, g_ref, w_ref, dy_ref, _dp_in, dp_ref, do_ref, dw_ref):
        i = pl.program_id(0)

        @pl.when(i == 0)
        def _():
            dw_ref[...] = jnp.zeros_like(dw_ref)
            dp_ref[...] = jnp.zeros_like(dp_ref)

        @pl.when(i >= 1)
        def _():
            o = o0_ref[...] + o1_ref[...]
            g = g_ref[...]
            w = w_ref[...]
            sg = _sig(g)
            dy_ = dy_ref[...]
            dsw = dy_ * (g * sg)
            outs, xhs = [], []
            for h in range(4):
                sl = slice(h * HGD, (h + 1) * HGD)
                oh = o[:, sl]
                r = lax.rsqrt(jnp.mean(oh * oh, axis=-1, keepdims=True) + EPS)
                xh = oh * r
                dxh = dsw[:, sl] * w[:, sl]
                outs.append(r * (dxh - xh * jnp.mean(dxh * xh, axis=-1, keepdims=True)))
                xhs.append(xh)
            xh = jnp.concatenate(xhs, axis=1)
            do_ref[...] = jnp.concatenate(outs, axis=1)
            dp_ref[...] = (dy_ * xh * w * (sg * (1.0 + g * (1.0 - sg)))).astype(BF16)
            dw_ref[...] += jnp.sum(dsw * xh, axis=0, keepdims=True)

    tok = pl.BlockSpec((TM, HGW), lambda i: (i, 0))
    lat = pl.BlockSpec((TM, HGW), lambda i: (jnp.maximum(i - 1, 0), 0))
    return _pcall(body, name="readout_bwd", grid=(tt // TM,),
                  in_specs=[tok, tok, pl.BlockSpec((TM, HGW), lambda i: (i, C_GHG)), _full((1, HGW)), lat, ANY],
                  out_specs=[pl.BlockSpec((TM, HGW), lambda i: (i, C_GHG)), lat, _full((1, HGW))],
                  out_shape=[jax.ShapeDtypeStruct(dp.shape, BF16), jax.ShapeDtypeStruct((s_len, HGW), F32),
                             jax.ShapeDtypeStruct((1, HGW), F32)],
                  aliases={5: 0}, carry=carry)(o0, o1, p, hw4, dy, dp)


def _rope_tables(s_len):
    t = np.arange(s_len)
    inv = ROPE_THETA ** (-np.arange(0, 32, 2, dtype=np.float64) / 32)
    def half(pos):
        ang = pos[:, None].astype(np.float64) * inv[None, :]
        return (np.concatenate([np.cos(ang), np.cos(ang)], 1), np.concatenate([-np.sin(ang), np.sin(ang)], 1))
    cr, sr = half(t // GRID_W)
    cc, sc = half(t % GRID_W)
    cos = np.concatenate([cr, cc, cr, cc], 1)
    sin = np.concatenate([sr, sc, sr, sc], 1)
    cos = np.concatenate([np.ones((L, 128)), cos], 0)
    sin = np.concatenate([np.zeros((L, 128)), sin], 0)
    return jnp.asarray(cos, F32), jnp.asarray(sin, F32)


def _blockdiag(n, w):
    i = np.arange(n)
    return jnp.asarray((i[:, None] // w == i[None, :] // w) / float(w), F32)


def _dup_matrix():
    m = np.zeros((128, 512), np.float32)
    for g in range(2):
        for j in range(4):
            for dd in range(HDIM):
                m[64 * g + dd, 256 * g + 64 * j + dd] = 1.0
    return m


def _head_mean(x, blockdiag):
    return _dot(x, blockdiag, prec=lax.Precision.HIGH)


def _rot(x):
    n = x.shape[1]
    lane = lax.broadcasted_iota(jnp.int32, x.shape, 1)
    return jnp.where((lane % 32) < 16, pltpu.roll(x, n - 16, 1), pltpu.roll(x, 16, 1))


def _qk_prep(p, cos, sin, qnw8, knw2, bd512, bd128, dup):
    tt = p.shape[0]

    def body(q_ref, kv_ref, cos_ref, sin_ref, qw_ref, kw_ref, b5_ref, b1_ref, dup_ref,
             qr_ref, k4_ref, v4_ref):
        cos_, sin_ = cos_ref[...], sin_ref[...]
        q = q_ref[...]
        qn = q * lax.rsqrt(_head_mean(q * q, b5_ref[...]) + EPS) * qw_ref[...]
        cos4 = jnp.concatenate([cos_] * 4, axis=1)
        sin4 = jnp.concatenate([sin_] * 4, axis=1)
        qr_ref[...] = ((qn * cos4 + _rot(qn) * sin4) * (HDIM ** -0.5)).astype(BF16)
        kv = kv_ref[...]
        k, v = kv[:, :128], kv[:, 128:]
        kn = k * lax.rsqrt(_head_mean(k * k, b1_ref[...]) + EPS) * kw_ref[...]
        kr = kn * cos_ + _rot(kn) * sin_
        k4_ref[...] = _bdot(kr, dup_ref[...]).astype(BF16)
        v4_ref[...] = _bdot(v, dup_ref[...]).astype(BF16)

    row = lambda w, cb: pl.BlockSpec((TM, w), lambda i: (i, cb))
    out = jax.ShapeDtypeStruct((tt, ATW), BF16)
    return _pcall(body, name="qk_prep", grid=(tt // TM,),
                  in_specs=[row(ATW, C_QRAW), row(256, C_KV), row(128, 0), row(128, 0),
                            _full((1, ATW)), _full((1, 128)), _full((ATW, ATW)), _full((128, 128)),
                            _full((128, ATW))],
                  out_specs=[row(ATW, 0)] * 3, out_shape=[out] * 3)(
                      p, p, cos, sin, qnw8, knw2, bd512, bd128, dup)


def _attn_masks(i, nb):
    r = lax.broadcasted_iota(jnp.int32, (4 * BLK, 3 * BLK + L), 0) % BLK
    c = lax.broadcasted_iota(jnp.int32, (4 * BLK, 3 * BLK + L), 1)
    kpos = (i - 1) * BLK + c
    loc = (jnp.abs(c - BLK - r) <= BLK) & (kpos >= 0) & (kpos < nb * BLK)
    return loc | (c >= 3 * BLK)


def _stack_mask():
    r = lax.broadcasted_iota(jnp.int32, (4 * BLK, 256), 0)
    lane = lax.broadcasted_iota(jnp.int32, (4 * BLK, 256), 1)
    return (r // BLK) == (lane // HDIM)


def _stack_heads(xg, fill=0.0):
    x4 = jnp.concatenate([xg] * 4, axis=0)
    return jnp.where(_stack_mask(), x4, jnp.full_like(x4, fill))


def _unstack_heads(x4):
    out = jnp.where(_lane_mask(0), x4[0:BLK], 0.0)
    for j in range(1, 4):
        out = out + jnp.where(_lane_mask(j), x4[j * BLK:(j + 1) * BLK], 0.0)
    return out


def _per_head_rows(vals):
    return jnp.concatenate([jnp.broadcast_to(v, (BLK, 1)) for v in vals], axis=0)


def _lane_mask(j):
    lane = lax.broadcasted_iota(jnp.int32, (1, 256), 1)
    return (lane // HDIM) == j


def _attn_specs(nb):
    blk = lambda off: pl.BlockSpec((BLK, ATW), lambda i: (jnp.clip(i + off, 0, nb - 1) + 2, 0))
    ctx = pl.BlockSpec((L, ATW), lambda i: (0, 0))
    return blk, ctx


def _attn_fwd(qr, k4, v4, sinks, carry=None):
    tt = qr.shape[0]
    s_len = tt - L
    nb = s_len // BLK

    def body(sk_ref, q_ref, kp, ko, kn, kc, vp, vo, vn, vc, y_ref, lse_ref):
        i = pl.program_id(0)
        valid = _attn_masks(i, nb)
        q = q_ref[...]
        ys, lses = [], []
        for g in range(2):
            gs = slice(256 * g, 256 * g + 256)
            kcat = jnp.concatenate([kp[:, gs], ko[:, gs], kn[:, gs], kc[:, gs]], axis=0)
            vcat = jnp.concatenate([vp[:, gs], vo[:, gs], vn[:, gs], vc[:, gs]], axis=0)
            sink = _per_head_rows([sk_ref[4 * g + j] for j in range(4)])
            s = jnp.where(valid, _dot(_stack_heads(q[:, gs]), kcat, NT), -1e30)
            m = jnp.maximum(jnp.max(s, axis=-1, keepdims=True), sink)
            e = jnp.exp(s - m)
            den = jnp.sum(e, axis=-1, keepdims=True) + jnp.exp(sink - m)
            ys.append(_unstack_heads(_bdot(e * (1.0 / den), vcat)))
            lses.append(_unstack_heads(jnp.broadcast_to(m + jnp.log(den), (4 * BLK, 256))))
        y_ref[...] = jnp.concatenate(ys, axis=1).astype(BF16)
        lse_ref[...] = jnp.concatenate(lses, axis=1)

    blk, ctx = _attn_specs(nb)
    out = pl.BlockSpec((BLK, ATW), lambda i: (i, 0))
    return _pcall(body, name="attn_fwd", grid=(nb,),
                  in_specs=[pl.BlockSpec(memory_space=pltpu.SMEM), blk(0),
                            blk(-1), blk(0), blk(1), ctx, blk(-1), blk(0), blk(1), ctx],
                  out_specs=[out, out],
                  out_shape=[jax.ShapeDtypeStruct((s_len, ATW), BF16),
                             jax.ShapeDtypeStruct((s_len, ATW), F32)], carry=carry)(
                      sinks, qr, k4, k4, k4, k4, v4, v4, v4, v4)


def _attn_bwd(qr, k4, v4, sinks, y, lse, dy, carry=None):
    tt = qr.shape[0]
    s_len = tt - L
    nb = s_len // BLK

    def body(sk_ref, q_ref, kp, ko, kn, kc, vp, vo, vn, vc, y_ref, lse_ref, dy_ref,
             dq_ref, dkw_ref, dvw_ref, dkc_ref, dvc_ref, dsk_ref):
        i = pl.program_id(0)

        @pl.when(i == 0)
        def _():
            dkc_ref[...] = jnp.zeros_like(dkc_ref)
            dvc_ref[...] = jnp.zeros_like(dvc_ref)
            dsk_ref[...] = jnp.zeros_like(dsk_ref)

        valid = _attn_masks(i, nb)
        q = q_ref[...]
        dy_ = dy_ref[...]
        dly = dy_ * y_ref[...].astype(F32)
        lse_ = lse_ref[...]
        dqs = []
        for g in range(2):
            gs = slice(256 * g, 256 * g + 256)
            kcat = jnp.concatenate([kp[:, gs], ko[:, gs], kn[:, gs], kc[:, gs]], axis=0)
            vcat = jnp.concatenate([vp[:, gs], vo[:, gs], vn[:, gs], vc[:, gs]], axis=0)
            q4 = _stack_heads(q[:, gs])
            dy4 = _stack_heads(dy_[:, gs]).astype(BF16)
            lse4 = jnp.max(_stack_heads(lse_[:, gs], fill=-1e30), axis=-1, keepdims=True)
            delta = jnp.sum(_stack_heads(dly[:, gs]), axis=-1, keepdims=True)
            sink = _per_head_rows([sk_ref[4 * g + j] for j in range(4)])
            pr = jnp.where(valid, jnp.exp(_dot(q4, kcat, NT) - lse4), 0.0)
            dsb = (pr * (_dot(dy4, vcat, NT) - delta)).astype(BF16)
            dsink = jnp.exp(sink - lse4) * delta
            for j in range(4):
                dsk_ref[4 * g + j:4 * g + j + 1, :] += jnp.broadcast_to(
                    -jnp.sum(dsink[j * BLK:(j + 1) * BLK], axis=0, keepdims=True), (1, 128))
            dqs.append(_unstack_heads(_dot(dsb, kcat)))
            dkg = _dot(dsb, q4, TN)
            dvg = _dot(pr.astype(BF16), dy4, TN)
            dkw_ref[0, :, gs] = dkg[:3 * BLK]
            dvw_ref[0, :, gs] = dvg[:3 * BLK]
            dkc_ref[:, gs] += dkg[3 * BLK:]
            dvc_ref[:, gs] += dvg[3 * BLK:]
        dq_ref[...] = jnp.concatenate(dqs, axis=1)

    blk, ctx = _attn_specs(nb)
    out = pl.BlockSpec((BLK, ATW), lambda i: (i, 0))
    win = pl.BlockSpec((1, 3 * BLK, ATW), lambda i: (i, 0, 0))
    acc = _full((L, ATW))
    return _pcall(body, name="attn_bwd", grid=(nb,),
                  in_specs=[pl.BlockSpec(memory_space=pltpu.SMEM), blk(0),
                            blk(-1), blk(0), blk(1), ctx, blk(-1), blk(0), blk(1), ctx, out, out, out],
                  out_specs=[out, win, win, acc, acc, _full((8, 128))],
                  out_shape=[jax.ShapeDtypeStruct((s_len, ATW), F32),
                             jax.ShapeDtypeStruct((nb, 3 * BLK, ATW), F32),
                             jax.ShapeDtypeStruct((nb, 3 * BLK, ATW), F32),
                             jax.ShapeDtypeStruct((L, ATW), F32), jax.ShapeDtypeStruct((L, ATW), F32),
                             jax.ShapeDtypeStruct((8, 128), F32)], carry=carry)(
                      sinks, qr, k4, k4, k4, k4, v4, v4, v4, v4, y, lse, dy)


def _attn_post(p, cos, sin, qnw8, knw2, bd512, bd128, dupt, dq, dkw, dvw, dkc, dvc, dp, carry=None):
    tt = p.shape[0]
    s_len = tt - L
    nb = s_len // BLK
    nctx = L // BLK

    def body(q_ref, kv_ref, cos_ref, sin_ref, qw_ref, kw_ref, b5_ref, b1_ref, dupt_ref,
             dq_ref, kwp, kwo, kwn, vwp, vwo, vwn, dkc_ref, dvc_ref, _dp_in,
             dp_ref, dqw_ref, dkw_ref):
        t = pl.program_id(0)
        j = t - nctx

        @pl.when(t == 0)
        def _():
            dqw_ref[...] = jnp.zeros_like(dqw_ref)
            dkw_ref[...] = jnp.zeros_like(dkw_ref)

        is_lat = t >= nctx
        cos_, sin_ = cos_ref[...], sin_ref[...]
        has_p = is_lat & (j >= 1)
        has_n = is_lat & (j <= nb - 2)
        dk4 = (jnp.where(is_lat, kwo[0], dkc_ref[...]) + jnp.where(has_p, kwp[0], 0.0)
               + jnp.where(has_n, kwn[0], 0.0))
        dv4 = (jnp.where(is_lat, vwo[0], dvc_ref[...]) + jnp.where(has_p, vwp[0], 0.0)
               + jnp.where(has_n, vwn[0], 0.0))
        dkr = _dot(dk4, dupt_ref[...], prec=HI)
        dv = _dot(dv4, dupt_ref[...], prec=HI)
        kv = kv_ref[...]
        k = kv[:, :128]
        kw = kw_ref[...]
        rk = lax.rsqrt(_head_mean(k * k, b1_ref[...]) + EPS)
        xk = k * rk
        dkn = dkr * cos_ + _rot(dkr * sin_)
        dxk = dkn * kw
        dk = rk * (dxk - xk * _head_mean(dxk * xk, b1_ref[...]))
        dkw_ref[...] += jnp.sum(dkn * xk, axis=0, keepdims=True)
        q = q_ref[...]
        qw = qw_ref[...]
        rq = lax.rsqrt(_head_mean(q * q, b5_ref[...]) + EPS)
        xq = q * rq
        cos4 = jnp.concatenate([cos_] * 4, axis=1)
        sin4 = jnp.concatenate([sin_] * 4, axis=1)
        dqr = jnp.where(is_lat, dq_ref[...], 0.0) * (HDIM ** -0.5)
        dqn = dqr * cos4 + _rot(dqr * sin4)
        dxq = dqn * qw
        dqraw = rq * (dxq - xq * _head_mean(dxq * xq, b5_ref[...]))
        dqw_ref[...] += jnp.sum(dqn * xq, axis=0, keepdims=True)
        dp_ref[...] = jnp.concatenate([dqraw, dk, dv], axis=1).astype(BF16)

    row = lambda w, cb: pl.BlockSpec((BLK, w), lambda t: (t, cb))
    lat = pl.BlockSpec((BLK, ATW), lambda t: (jnp.maximum(t - nctx, 0), 0))

    def part(off):
        return pl.BlockSpec((1, BLK, ATW), lambda t: (jnp.clip(t - nctx + off, 0, nb - 1), 1 - off, 0))

    cacc = pl.BlockSpec((BLK, ATW), lambda t: (jnp.minimum(t, nctx - 1), 0))
    return _pcall(body, name="attn_post", grid=(tt // BLK,),
                  in_specs=[row(ATW, C_QRAW), row(256, C_KV), row(128, 0), row(128, 0),
                            _full((1, ATW)), _full((1, 128)), _full((ATW, ATW)), _full((128, 128)),
                            _full((ATW, 128)), lat, part(-1), part(0), part(1), part(-1), part(0), part(1),
                            cacc, cacc, ANY],
                  out_specs=[pl.BlockSpec((BLK, 768), lambda t: (t, C_QKV)), _full((1, ATW)), _full((1, 128))],
                  out_shape=[jax.ShapeDtypeStruct(dp.shape, BF16), jax.ShapeDtypeStruct((1, ATW), F32),
                             jax.ShapeDtypeStruct((1, 128), F32)],
                  aliases={18: 0}, carry=carry)(p, p, cos, sin, qnw8, knw2, bd512, bd128, dupt,
                                   dq, dkw, dkw, dkw, dvw, dvw, dvw, dkc, dvc, dp)


def _branch_merge(y_hg, y_at, bh4, ba4, p):
    s_len = y_hg.shape[0]

    def body(yh_ref, ya_ref, bh_ref, ba_ref, gh_ref, ga_ref, ah_ref, aa_ref, m_ref):
        yh, ya = yh_ref[...], ya_ref[...]
        ah = jnp.concatenate([_bdot(yh, bh_ref[j]) for j in range(4)], axis=1)
        aa = jnp.concatenate([_bdot(ya, ba_ref[j]) for j in range(4)], axis=1)
        ah_ref[...] = ah
        aa_ref[...] = aa
        m_ref[...] = (_sig(gh_ref[...]) * ah + _sig(ga_ref[...]) * aa).astype(BF16)

    row = pl.BlockSpec((TM, D), lambda i: (i, 0))
    y = pl.BlockSpec((TM, HGW), lambda i: (i, 0))
    f = jax.ShapeDtypeStruct((s_len, D), F32)
    return _pcall(body, name="branch_merge", grid=(s_len // TM,),
                  in_specs=[y, y, _full(bh4.shape), _full(ba4.shape),
                            pl.BlockSpec((TM, D), lambda i: (i + 1, 2)), pl.BlockSpec((TM, D), lambda i: (i + 1, 3))],
                  out_specs=[row, row, row],
                  out_shape=[f, f, jax.ShapeDtypeStruct((s_len, D), BF16)])(y_hg, y_at, bh4, ba4, p, p)


def _branch_bwd(dmh, dma, bh4, ba4, y_hg, y_at):
    s_len = dmh.shape[0]
    nk = s_len // TS
    ns = D // 4

    def body(dh_ref, da_ref, bh_ref, ba_ref, yh_ref, ya_ref, dyh_ref, dya_ref, gh_ref, ga_ref, acc_h, acc_a):
        t = pl.program_id(0)

        @pl.when(t == 0)
        def _():
            acc_h[...] = jnp.zeros_like(acc_h)
            acc_a[...] = jnp.zeros_like(acc_a)

        for d_ref, w_ref, y_ref, dy_ref, acc in ((dh_ref, bh_ref, yh_ref, dyh_ref, acc_h),
                                                 (da_ref, ba_ref, ya_ref, dya_ref, acc_a)):
            y = y_ref[...]
            dy = jnp.zeros((TS, HGW), F32)
            for j in range(4):
                dj = d_ref[:, j * ns:(j + 1) * ns]
                dy = dy + _bdot(dj, w_ref[j], NT)
                acc[j] += _bdot(y, dj, TN)
            dy_ref[...] = dy

        @pl.when(t == nk - 1)
        def _():
            gh_ref[...] = acc_h[...].astype(BF16)
            ga_ref[...] = acc_a[...].astype(BF16)

    dm = pl.BlockSpec((TS, D), lambda t: (t, 0))
    y = pl.BlockSpec((TS, HGW), lambda t: (t, 0))
    w = _full(bh4.shape)
    fy = jax.ShapeDtypeStruct((s_len, HGW), F32)
    gw = jax.ShapeDtypeStruct(bh4.shape, BF16)
    return _pcall(body, name="branch_bwd", grid=(nk,), in_specs=[dm, dm, w, w, y, y],
                  out_specs=[y, y, w, w], out_shape=[fy, fy, gw, gw],
                  scratch=[pltpu.VMEM(bh4.shape, F32)] * 2)(dmh, dma, bh4, ba4, y_hg, y_at)


def _merge_bwd(dattn, w_o, mixed, ah, aa, p, carry=None):
    tt = p.shape[0]
    s_len = tt - L
    nt = tt // TM

    def body(da_ref, wo_ref, mx_ref, ah_ref, aa_ref, gh_ref, ga_ref, dp_ref, dmh_ref, dma_ref, go_ref, acc):
        i = pl.program_id(0)

        @pl.when(i == 0)
        def _():
            dp_ref[...] = jnp.zeros_like(dp_ref)
            acc[...] = jnp.zeros_like(acc)

        @pl.when(i >= 1)
        def _():
            da = da_ref[...]
            acc[...] += _bdot(mx_ref[...], da, TN)
            dm_ = _bdot(da, wo_ref[...], NT)
            sh, sa = _sig(gh_ref[...]), _sig(ga_ref[...])
            dp_ref[...] = jnp.concatenate([dm_ * ah_ref[...] * sh * (1.0 - sh),
                                           dm_ * aa_ref[...] * sa * (1.0 - sa)], axis=1).astype(BF16)
            dmh_ref[...] = (dm_ * sh).astype(BF16)
            dma_ref[...] = (dm_ * sa).astype(BF16)

        @pl.when(i == nt - 1)
        def _():
            go_ref[...] = acc[...].astype(BF16)

    lat = pl.BlockSpec((TM, D), lambda i: (jnp.maximum(i - 1, 0), 0))
    return _pcall(body, name="merge_bwd", grid=(nt,),
                  in_specs=[lat, _full((D, D)), lat, lat, lat, pl.BlockSpec((TM, D), lambda i: (i, 2)),
                            pl.BlockSpec((TM, D), lambda i: (i, 3))],
                  out_specs=[pl.BlockSpec((TM, 2 * D), lambda i: (i, C_GATES)), lat, lat, _full((D, D))],
                  out_shape=[jax.ShapeDtypeStruct((tt, NCOL), BF16), jax.ShapeDtypeStruct((s_len, D), BF16),
                             jax.ShapeDtypeStruct((s_len, D), BF16), jax.ShapeDtypeStruct((D, D), BF16)],
                  scratch=[pltpu.VMEM((D, D), F32)], carry=carry)(dattn, w_o, mixed, ah, aa, p, p)


def _local_step(x, ctx, tgt, mod, modc, nw1, nw2, lg, hw, qnw, knw, sinks,
                w_in, wts, dist=None):
    s_len = x.shape[0]
    tt = s_len + L
    ss1 = jnp.stack([modc, mod[0:2]])
    ss2 = mod[3:5][None]
    g1, g2 = mod[2:3], mod[5:6]
    hw4 = jnp.tile(hw, (1, 4))
    qnw8 = jnp.tile(qnw, (1, 8))
    knw2 = jnp.tile(knw, (1, 2))
    cos, sin = _rope_tables(s_len)
    bd512, bd128 = _blockdiag(ATW, HDIM), _blockdiag(128, HDIM)
    dupm = _dup_matrix()
    dup, dupt = jnp.asarray(dupm, BF16), jnp.asarray(dupm.T, F32)
    tmt = tt

    def four(b):
        return b.reshape(4, 2 * b.shape[1], b.shape[2])

    def halves(g):
        return g.reshape(4, 2, g.shape[1] // 2, g.shape[2])

    h = _mod1(ctx, x, nw1, ss1)
    if dist is None:
        bh4, ba4, w_o, g4, u4, dn4 = wts
        p = _mm_in(h, w_in, tmt)
        o0, st0 = _hgrn_fwd(p, lg, rev=False)
        o1, st1 = _hgrn_fwd(p, lg, rev=True)
    else:
        core, chip = dist
        half = wts[3].shape[1] // 2
        p, first = _mm_in(h, w_in, tmt, carry=_carry_join(_carry_gather(list(wts[0:3])),
                                                          _carry_gather([wts[3]], rows=[(0, half)])))
        (o0, st0), (g8,) = _hgrn_fwd(p, lg, rev=False, carry=_carry_gather([first[3]], rows=[(half, half)]))
        (o1, st1), (dn8a,) = _hgrn_fwd(p, lg, rev=True, carry=_carry_gather([wts[5]], rows=[(0, half)]))
        bh4, ba4, w_o, g4 = four(first[0]), four(first[1]), four(first[2]).reshape(D, D), four(g8)
    y_hg = _readout(o0, o1, p, hw4)
    qr, k4, v4 = _qk_prep(p, cos, sin, qnw8, knw2, bd512, bd128, dup)
    if dist is None:
        y_at, lse = _attn_fwd(qr, k4, v4, sinks)
    else:
        (y_at, lse), (u8, dn8) = _attn_fwd(qr, k4, v4, sinks,
                                           carry=_carry_gather([wts[4], dn8a], rows=[None, (half, half)]))
        u4, dn4 = four(u8), four(dn8)
    ah, aa, mixed = _branch_merge(y_hg, y_at, bh4, ba4, p)
    ao, x1, h2 = _out_proj_mod2(mixed, w_o, x, g1, nw2, ss2)
    a4, b4, z4 = _ffn_up(h2, g4, u4)
    sq, dx2, dyb, dg2 = _ffn_down_loss(z4, dn4, x1, g2, tgt)

    da4, db4 = _ffn_dz(dyb, dn4, a4, b4)
    g_dn = _ffn_gdn(z4, dyb)
    if dist is None:
        dh2 = _ffn_dh2(da4, db4, g4, u4)
    else:
        dn_units = [halves(g_dn)]
        dh2, dn_recv = _ffn_dh2(da4, db4, g4, u4, carry=_carry_pairx(dn_units))
        dn_pairs = _rs_pair_add(dn_units, dn_recv, core)
    if dist is None:
        g_g, g_u = _ffn_ggu(h2, da4, db4)
    else:
        (g_g, g_u), c_dn = _ffn_ggu(h2, da4, db4, carry=_carry_chipx(dn_pairs))
        red_dn = _rs_chip_add(dn_pairs, c_dn, core, chip)
    dx1, dattn, dss2, dnw2, dg1 = _mod2_bwd(x1, dh2, dx2, ao, nw2, ss2, g1)
    if dist is None:
        dp, dmh, dma, g_o = _merge_bwd(dattn, w_o, mixed, ah, aa, p)
    else:
        gu_units = [halves(g_g), halves(g_u)]
        (dp, dmh, dma, g_o), gu_recv = _merge_bwd(dattn, w_o, mixed, ah, aa, p, carry=_carry_pairx(gu_units))
        ffn_pairs = list(dn_pairs) + list(_rs_pair_add(gu_units, gu_recv, core))
    dy_hg, dy_at, g_bh, g_ba = _branch_bwd(dmh, dma, bh4, ba4, y_hg, y_at)
    if dist is None:
        dp, do, dhw4 = _readout_bwd(o0, o1, p, hw4, dy_hg, dp)
        dq, dkw, dvw, dkc, dvc, dsk = _attn_bwd(qr, k4, v4, sinks, y_at, lse, dy_at)
        dp, dqnw8, dknw2 = _attn_post(p, cos, sin, qnw8, knw2, bd512, bd128, dupt, dq, dkw, dvw, dkc, dvc, dp)
    else:
        mix_units = [halves(g_bh), halves(g_ba), halves(g_o.reshape(4, D // 4, D))]
        (dp, do, dhw4), mix_recv = _readout_bwd(o0, o1, p, hw4, dy_hg, dp, carry=_carry_pairx(mix_units))
        mix_pairs = _rs_pair_add(mix_units, mix_recv, core)
        (dq, dkw, dvw, dkc, dvc, dsk), bwd = _attn_bwd(
            qr, k4, v4, sinks, y_at, lse, dy_at,
            carry=_carry_join(_carry_chipx(ffn_pairs[1:2]), _carry_sibx(red_dn)))
        red_g = _rs_chip_add(ffn_pairs[1:2], bwd[0:1], core, chip)
        (dp, dqnw8, dknw2), post = _attn_post(
            p, cos, sin, qnw8, knw2, bd512, bd128, dupt, dq, dkw, dvw, dkc, dvc, dp, carry=_carry_sibx(red_g))
    if dist is None:
        dp, dv0, dq0, dlg0 = _hgrn_bwd(p, lg, do, st0, dp, None, rev=False)
        dp, dlg1 = _hgrn_bwd(p, lg, do, st1, dp, (dv0, dq0), rev=True)
    else:
        (dp, dv0, dq0, dlg0), c_u = _hgrn_bwd(p, lg, do, st0, dp, None, rev=False,
                                              carry=_carry_chipx(ffn_pairs[2:3]))
        red_u = _rs_chip_add(ffn_pairs[2:3], c_u, core, chip)
        (dp, dlg1), last = _hgrn_bwd(p, lg, do, st1, dp, (dv0, dq0), rev=True,
                                     carry=_carry_join(_carry_chipx(mix_pairs), _carry_sibx(red_u)))
        mix_reds = _rs_chip_add(mix_pairs, last[0:3], core, chip)
        ffn_done = bwd[1:2] + post[0:1] + last[3:4]
    g_in = _mm_gin(dp, h, tmt)
    if dist is None:
        dh = _mm_dh(dp, w_in, tmt)
        gx, dss1, dnw1 = _mod1_bwd(ctx, x, dh, dx1, nw1, ss1)
        rs = None
    else:
        in_units = [halves(g_in.reshape(4, NCOL // 4, D))]
        dh, both = _mm_dh(dp, w_in, tmt, carry=_carry_join(_carry_pairx(in_units), _carry_sibx(mix_reds)))
        in_recv, mix_done = both[0:1], both[1:4]
        in_pairs = _rs_pair_add(in_units, in_recv, core)
        first_rows = (0, in_pairs[0].shape[1] // 2)
        (gx, dss1, dnw1), in_part = _mod1_bwd(ctx, x, dh, dx1, nw1, ss1,
                                              carry=_carry_chipx(in_pairs, rows=first_rows))
        rs = dict(ffn_done=ffn_done, mix_done=mix_done, in_pairs=in_pairs, in_part=in_part)

    dmod = jnp.concatenate([dss1[1], dg1, dss2, dg2], axis=0)
    dmodc = dss1[0]
    raw = (dss1, dg1, dss2, dg2, dnw1, dnw2, dhw4, dqnw8, dknw2, dsk, dlg0, dlg1)
    small = dict(raw=raw, dmod=dmod, dmodc=dmodc, dnw1=dnw1, dnw2=dnw2,
                 dhw=dhw4.reshape(4, HGD).sum(0, keepdims=True),
                 dqnw=dqnw8.reshape(8, HDIM).sum(0, keepdims=True),
                 dknw=dknw2.reshape(2, HDIM).sum(0, keepdims=True),
                 dsinks=dsk[:, 0], dlg=jnp.concatenate([dlg0, dlg1], axis=0))
    big = dict(w_in=g_in, w_bh=g_bh, w_ba=g_ba, w_o=g_o, w_g=g_g, w_u=g_u, w_dn=g_dn)
    return sq, gx, big, small, rs


def _place():
    x, y, c = lax.axis_index("x"), lax.axis_index("y"), lax.axis_index("c")
    return x, y, c


def _gather_blocks(x_refs, out_refs, send_sems, recv_sems, local_sems):
    n = len(out_refs)
    x, y, c = _place()
    me, sibling = (x, y, c), (x, y, 1 - c)
    chips = [(1 - x, y), (x, 1 - y), (1 - x, 1 - y)]

    def slot(u, px, py, pc):
        return out_refs[u].at[4 * px + 2 * py + pc]

    def copy(u, k, block, to, src=None):
        return pltpu.make_async_remote_copy(
            src_ref=slot(u, *block) if src is None else src, dst_ref=slot(u, *block),
            send_sem=send_sems.at[u, k], recv_sem=recv_sems.at[u, k], device_id=to, device_id_type=MESH)

    mines = []
    if x_refs is not None:
        mines = [pltpu.make_async_copy(x_refs[u], slot(u, *me), local_sems.at[u]) for u in range(n)]
    for cp in mines:
        cp.start()
    first = []
    for u in range(n):
        src = None if x_refs is None else x_refs[u]
        first.append(copy(u, 0, me, sibling, src=src))
        first += [copy(u, 1 + j, me, (*chip, c), src=src) for j, chip in enumerate(chips)]
    for cp in first:
        cp.start()
    passed = []
    for j, chip in enumerate(chips):
        for u in range(n):
            copy(u, 1 + j, (*chip, c), me).wait_recv()
            fwd = copy(u, 4 + j, (*chip, c), sibling)
            fwd.start()
            passed.append(fwd)
    for u in range(n):
        copy(u, 0, sibling, me).wait_recv()
    for j, chip in enumerate(chips):
        for u in range(n):
            copy(u, 4 + j, (*chip, 1 - c), me).wait_recv()
    for cp in first + passed:
        cp.wait_send()
    for cp in mines:
        cp.wait()


def _gather_sems(n):
    return [pltpu.SemaphoreType.DMA((n, 7)), pltpu.SemaphoreType.DMA((n, 7)), pltpu.SemaphoreType.DMA((n,))]


def _allgather(blks, *, name, in_vmem):
    n = len(blks)
    space = pltpu.VMEM if in_vmem else pl.ANY

    def body(*refs):
        _gather_blocks(refs[:n], refs[n:2 * n], *refs[2 * n:])

    return pl.pallas_call(
        body, name=name, out_shape=[jax.ShapeDtypeStruct((8,) + b.shape, b.dtype) for b in blks],
        in_specs=[pl.BlockSpec(memory_space=space)] * n, out_specs=[pl.BlockSpec(memory_space=space)] * n,
        scratch_shapes=_gather_sems(n))(*blks)


def _cast_place(ws, c, dev):
    n = len(ws)

    def body(s_ref, *refs):
        for u in range(n):
            refs[n + u][0] = refs[u][...].astype(BF16)

    in_specs, out_specs, out_shape = [], [], []
    for w in ws:
        q, cols = w.shape[0] // 4, w.shape[1]
        in_specs.append(pl.BlockSpec((q, cols), lambda i, s: (2 * s[0] + i, 0)))
        out_specs.append(pl.BlockSpec((1, q, cols), lambda i, s: (s[1], i, 0)))
        out_shape.append(jax.ShapeDtypeStruct((8, 2 * q, cols), BF16))
    return pl.pallas_call(
        body, name="cast_place",
        grid_spec=pltpu.PrefetchScalarGridSpec(num_scalar_prefetch=1, grid=(2,), in_specs=in_specs,
                                               out_specs=out_specs),
        out_shape=_out_hbm(out_shape),
        compiler_params=pltpu.CompilerParams(vmem_limit_bytes=48 << 20))(jnp.stack([c, dev]), *_in_hbm(ws))


def _gather_phases(out_refs, send_sems, recv_sems, rows=None):
    n = len(out_refs)
    x, y, c = _place()
    me, sibling = (x, y, c), (x, y, 1 - c)
    chips = [(1 - x, y), (x, 1 - y), (1 - x, 1 - y)]

    def copy(u, k, block, to):
        px, py, pc = block
        ref = out_refs[u].at[4 * px + 2 * py + pc]
        if rows is not None and rows[u] is not None:
            ref = ref.at[pl.ds(rows[u][0], rows[u][1])]
        return pltpu.make_async_remote_copy(src_ref=ref, dst_ref=ref, send_sem=send_sems.at[u, k],
                                            recv_sem=recv_sems.at[u, k], device_id=to, device_id_type=MESH)

    def start():
        for u in range(n):
            copy(u, 0, me, sibling).start()
            for j, chip in enumerate(chips):
                copy(u, 1 + j, me, (*chip, c)).start()

    def mid():
        for j, chip in enumerate(chips):
            for u in range(n):
                copy(u, 1 + j, (*chip, c), me).wait_recv()
                copy(u, 4 + j, (*chip, c), sibling).start()

    def end():
        for u in range(n):
            copy(u, 0, sibling, me).wait_recv()
        for j, chip in enumerate(chips):
            for u in range(n):
                copy(u, 4 + j, (*chip, 1 - c), me).wait_recv()
        for u in range(n):
            copy(u, 0, me, sibling).wait_send()
            for j, chip in enumerate(chips):
                copy(u, 1 + j, me, (*chip, c)).wait_send()
                copy(u, 4 + j, (*chip, c), sibling).wait_send()

    return start, mid, end


def _carry_gather(bufs, rows=None):
    n = len(bufs)
    return _Carry(bufs, [jax.ShapeDtypeStruct(b.shape, b.dtype) for b in bufs], {u: u for u in range(n)},
                  [pltpu.SemaphoreType.DMA((n, 7)), pltpu.SemaphoreType.DMA((n, 7))],
                  lambda ins, outs, sems: _gather_phases(outs, *sems, rows=rows))


def _allgather_inplace(bufs, *, name):
    n = len(bufs)

    def body(*refs):
        for phase in _gather_phases(refs[n:2 * n], *refs[2 * n:]):
            phase()

    return pl.pallas_call(
        body, name=name, out_shape=[jax.ShapeDtypeStruct(b.shape, b.dtype) for b in bufs],
        in_specs=[ANY] * n, out_specs=[ANY] * n, input_output_aliases={u: u for u in range(n)},
        scratch_shapes=[pltpu.SemaphoreType.DMA((n, 7)), pltpu.SemaphoreType.DMA((n, 7))])(*bufs)


def _ag_small(raw, sq):
    def body(dss1, dg1, dss2, dg2, dnw1, dnw2, dhw4, dqnw8, dknw2, dsk, dlg0, dlg1, sq_ref,
             out_ref, tot_ref, blk, send_sems, recv_sems, local_sems):
        blk[...] = jnp.zeros_like(blk)
        blk[0:2, :] = dss1[1]
        blk[2:3, :] = dg1[...]
        blk[3:5, :] = dss2[...]
        blk[5:6, :] = dg2[...]
        blk[6:8, :] = dss1[0]
        blk[8:9, :] = dnw1[...]
        blk[9:10, :] = dnw2[...]
        blk[10:11, 0:HGW] = dhw4[...]
        blk[10:11, HGW:D] = dqnw8[...]
        blk[11:12, 0:128] = dknw2[...]
        blk[12:14, 0:HGW] = dlg0[0]
        blk[14:16, 0:HGW] = dlg1[0]
        blk[16:24, 0:128] = dsk[...]
        blk[24:25, :] = sq_ref[...]
        _gather_blocks([blk], [out_ref], send_sems, recv_sems, local_sems)
        acc = out_ref[0]
        for i in range(1, 8):
            acc = acc + out_ref[i]
        tot_ref[...] = acc

    vm = pl.BlockSpec(memory_space=pltpu.VMEM)
    return pl.pallas_call(
        body, name="ag_small",
        out_shape=[jax.ShapeDtypeStruct((8, 32, D), F32), jax.ShapeDtypeStruct((32, D), F32)],
        in_specs=[vm] * 13, out_specs=[vm, vm],
        scratch_shapes=[pltpu.VMEM((32, D), F32)] + _gather_sems(1))(*raw, sq)


def _rs_pair_exchange(units):
    n = len(units)

    def body(*refs):
        start, _, end = _pairx_phases(refs[:n], refs[n:2 * n], *refs[2 * n:])
        start()
        end()

    return pl.pallas_call(
        body, name="rs_pair_exchange", out_shape=_pairx_shapes(units),
        in_specs=[ANY] * n, out_specs=[ANY] * n,
        scratch_shapes=[pltpu.SemaphoreType.DMA((n, 4)), pltpu.SemaphoreType.DMA((n, 4))])(*units)


def _pairx_shapes(units):
    return [jax.ShapeDtypeStruct((4,) + g.shape[2:], g.dtype) for g in units]


def _pairx_phases(g_refs, r_refs, send_sems, recv_sems):
    n = len(g_refs)
    x, y, c = _place()
    cps = [pltpu.make_async_remote_copy(
        src_ref=g_refs[u].at[j, 1 - c], dst_ref=r_refs[u].at[j], send_sem=send_sems.at[u, j],
        recv_sem=recv_sems.at[u, j], device_id=(x, y, 1 - c), device_id_type=MESH)
        for u in range(n) for j in range(4)]

    def start():
        for cp in cps:
            cp.start()

    def end():
        for cp in cps:
            cp.wait()

    return start, None, end


def _carry_pairx(units):
    n = len(units)
    return _Carry(units, _pairx_shapes(units), {},
                  [pltpu.SemaphoreType.DMA((n, 4)), pltpu.SemaphoreType.DMA((n, 4))],
                  lambda ins, outs, sems: _pairx_phases(ins, outs, *sems))


def _rs_pair_add(units, recvs, c):
    n = len(units)

    def body(c_ref, *refs):
        for u in range(n):
            refs[2 * n + u][...] = (refs[u][0].astype(F32) + refs[n + u][...].astype(F32)).astype(BF16)

    in_specs, out_specs, out_shape = [], [], []
    for g in units:
        h, w = g.shape[2] // 2, g.shape[3]
        in_specs.append(pl.BlockSpec((1, 1, h, w), lambda j, i, cr: (j, cr[0], i, 0)))
    for g in units:
        h, w = g.shape[2] // 2, g.shape[3]
        in_specs.append(pl.BlockSpec((1, h, w), lambda j, i, cr: (j, i, 0)))
        out_specs.append(pl.BlockSpec((1, h, w), lambda j, i, cr: (j, i, 0)))
        out_shape.append(jax.ShapeDtypeStruct((4, 2 * h, w), BF16))
    return pl.pallas_call(
        body, name="rs_pair_add",
        grid_spec=pltpu.PrefetchScalarGridSpec(num_scalar_prefetch=1, grid=(4, 2), in_specs=in_specs,
                                               out_specs=out_specs),
        out_shape=_out_hbm(out_shape),
        compiler_params=pltpu.CompilerParams(vmem_limit_bytes=48 << 20))(
            c.reshape(1), *_in_hbm(list(units) + list(recvs)))


def _rs_chip_exchange(pairs):
    n = len(pairs)

    def body(*refs):
        start, _, end = _chipx_phases(refs[:n], refs[n:2 * n], *refs[2 * n:])
        start()
        end()

    return pl.pallas_call(
        body, name="rs_chip_exchange", out_shape=[jax.ShapeDtypeStruct(p.shape, p.dtype) for p in pairs],
        in_specs=[ANY] * n, out_specs=[ANY] * n,
        scratch_shapes=[pltpu.SemaphoreType.DMA((n, 3)), pltpu.SemaphoreType.DMA((n, 3))])(*pairs)


def _chipx_phases(p_refs, r_refs, send_sems, recv_sems, rows=None):
    n = len(p_refs)
    x, y, c = _place()
    k = 2 * x + y

    def part(ref):
        return ref if rows is None else ref.at[pl.ds(rows[0], rows[1])]

    sends = []
    for d in range(1, 4):
        j = (k + d) % 4
        for u in range(n):
            sends.append(pltpu.make_async_remote_copy(
                src_ref=part(p_refs[u].at[j]), dst_ref=part(r_refs[u].at[k]), send_sem=send_sems.at[u, d - 1],
                recv_sem=recv_sems.at[u, d - 1], device_id=(j // 2, j % 2, c), device_id_type=MESH))

    def start():
        for cp in sends:
            cp.start()

    def end():
        for d in range(1, 4):
            src = (k + 4 - d) % 4
            for u in range(n):
                pltpu.make_async_remote_copy(
                    src_ref=part(p_refs[u].at[src]), dst_ref=part(r_refs[u].at[src]),
                    send_sem=send_sems.at[u, d - 1], recv_sem=recv_sems.at[u, d - 1], device_id=(x, y, c),
                    device_id_type=MESH).wait_recv()
        for cp in sends:
            cp.wait_send()

    return start, None, end


def _carry_chipx(pairs, rows=None, into=None):
    n = len(pairs)
    sems = [pltpu.SemaphoreType.DMA((n, 3)), pltpu.SemaphoreType.DMA((n, 3))]
    shapes = [jax.ShapeDtypeStruct(p.shape, p.dtype) for p in pairs]
    if into is None:
        return _Carry(pairs, shapes, {}, sems, lambda ins, outs, s: _chipx_phases(ins, outs, *s, rows=rows))
    return _Carry(list(pairs) + list(into), shapes, {n + u: u for u in range(n)}, sems,
                  lambda ins, outs, s: _chipx_phases(ins[:n], outs, *s, rows=rows))


def _rs_chip_add(pairs, contribs, c, chip):
    n = len(pairs)

    def body(s_ref, *refs):
        for u in range(n):
            a, b, c_, d = refs[4 * u:4 * u + 4]
            refs[4 * n + u][0] = ((a[0].astype(F32) + b[0].astype(F32)) + c_[0].astype(F32)) + d[0].astype(F32)

    in_specs, out_specs, out_shape, args = [], [], [], []
    for p, r in zip(pairs, contribs):
        h, w = p.shape[1] // 2, p.shape[2]
        in_specs += [pl.BlockSpec((1, h, w), functools.partial(lambda d, i, s: ((s[1] + d) % 4, i, 0), d))
                     for d in range(4)]
        args += [p, r, r, r]
        out_specs.append(pl.BlockSpec((1, h, w), lambda i, s: (s[0], i, 0)))
        out_shape.append(jax.ShapeDtypeStruct((2, 2 * h, w), F32))
    return pl.pallas_call(
        body, name="rs_chip_add",
        grid_spec=pltpu.PrefetchScalarGridSpec(num_scalar_prefetch=1, grid=(2,), in_specs=in_specs,
                                               out_specs=out_specs),
        out_shape=_out_hbm(out_shape),
        compiler_params=pltpu.CompilerParams(vmem_limit_bytes=48 << 20))(jnp.stack([c, chip]), *_in_hbm(args))


def _rs_sibling_gather(reds):
    n = len(reds)

    def body(*refs):
        start, _, end = _sibx_phases(refs[n:2 * n], *refs[2 * n:])
        start()
        end()

    return pl.pallas_call(
        body, name="rs_sibling_gather", out_shape=[jax.ShapeDtypeStruct(r.shape, r.dtype) for r in reds],
        in_specs=[ANY] * n, out_specs=[ANY] * n, input_output_aliases={u: u for u in range(n)},
        scratch_shapes=[pltpu.SemaphoreType.DMA((n,))] * 2)(*reds)


def _sibx_phases(o_refs, send_sems, recv_sems):
    n = len(o_refs)
    x, y, c = _place()
    cps = [pltpu.make_async_remote_copy(
        src_ref=o_refs[u].at[c], dst_ref=o_refs[u].at[c], send_sem=send_sems.at[u], recv_sem=recv_sems.at[u],
        device_id=(x, y, 1 - c), device_id_type=MESH) for u in range(n)]

    def start():
        for cp in cps:
            cp.start()

    def end():
        for u in range(n):
            cps[u].wait_send()
            pltpu.make_async_remote_copy(
                src_ref=o_refs[u].at[1 - c], dst_ref=o_refs[u].at[1 - c], send_sem=send_sems.at[u],
                recv_sem=recv_sems.at[u], device_id=(x, y, 1 - c), device_id_type=MESH).wait_recv()

    return start, None, end


def _carry_sibx(reds):
    n = len(reds)
    return _Carry(reds, [jax.ShapeDtypeStruct(r.shape, r.dtype) for r in reds], {u: u for u in range(n)},
                  [pltpu.SemaphoreType.DMA((n,))] * 2, lambda ins, outs, sems: _sibx_phases(outs, *sems))


def _prologue(blk, c_ctx, w, b, in8):
    n = w.shape[1]

    def body(blk_ref, cctx_ref, w_ref, b_ref, _in_in, g0_ref, c16_ref, g1_ref, in_ref, mod_s,
             s1, r1, l1, s2, r2, l2, s3, r3):
        start, mid, end = _gather_phases([in_ref], s3, r3)
        start()
        _gather_blocks([blk_ref], [g0_ref], s1, r1, l1)
        c16 = jnp.concatenate([g0_ref[i, 0:1, :] for i in range(8)] + [cctx_ref[...], jnp.zeros((7, D), F32)],
                              axis=0)
        c16_ref[...] = c16
        mod_s[...] = _dot(c16 * _sig(c16), w_ref[...], prec=HI) + b_ref[...]
        _gather_blocks([mod_s], [g1_ref], s2, r2, l2)
        mid()
        end()

    vm = pl.BlockSpec(memory_space=pltpu.VMEM)
    return pl.pallas_call(
        body, name="prologue",
        out_shape=[jax.ShapeDtypeStruct((8, 8, D), F32), jax.ShapeDtypeStruct((16, D), F32),
                   jax.ShapeDtypeStruct((8, 16, n), F32), jax.ShapeDtypeStruct(in8.shape, in8.dtype)],
        in_specs=[vm, vm, vm, vm, ANY], out_specs=[vm, vm, vm, ANY], input_output_aliases={4: 3},
        scratch_shapes=[pltpu.VMEM((16, n), F32)] + _gather_sems(1) + _gather_sems(1)
        + [pltpu.SemaphoreType.DMA((1, 7)), pltpu.SemaphoreType.DMA((1, 7))],
        compiler_params=pltpu.CompilerParams(vmem_limit_bytes=48 << 20))(blk, c_ctx, w, b, in8)


def _ada_bwd(c16, dmod16, w):
    n = w.shape[1]
    tn = 512

    def body(c_ref, d_ref, w_ref, gw_ref, gc_ref):
        j = pl.program_id(0)

        @pl.when(j == 0)
        def _():
            gc_ref[...] = jnp.zeros_like(gc_ref)

        cc = c_ref[...]
        dm = d_ref[...]
        gw_ref[...] = _dot(cc * _sig(cc), dm, TN, prec=HI)
        gc_ref[...] += _dot(dm, w_ref[...], NT, prec=HI)

    return _pcall(body, name="ada_bwd", grid=(n // tn,),
                  in_specs=[_full((16, D)), pl.BlockSpec((16, tn), lambda j: (0, j)),
                            pl.BlockSpec((D, tn), lambda j: (0, j))],
                  out_specs=[pl.BlockSpec((D, tn), lambda j: (0, j)), _full((16, D))],
                  out_shape=[jax.ShapeDtypeStruct((D, n), F32),
                             jax.ShapeDtypeStruct((16, D), F32)])(c16, dmod16, w)


def _adam_math(w, g, m, v):
    c1 = 1.0 - ADAM_B1 ** ADAM_STEP
    c2 = 1.0 - ADAM_B2 ** ADAM_STEP
    nm = ADAM_B1 * m + (1.0 - ADAM_B1) * g
    nv = ADAM_B2 * v + (1.0 - ADAM_B2) * (g * g)
    return -ADAM_LR * ((nm / c1) / (jnp.sqrt(nv / c2) + ADAM_EPS) + ADAM_WD * w), nm, nv


def _adamw_small(ws, gs, ms, vs):
    n = len(ws)

    def body(*refs):
        for u in range(n):
            d_, nm, nv = _adam_math(refs[u][...], refs[n + u][...], refs[2 * n + u][...], refs[3 * n + u][...])
            refs[4 * n + u][...] = d_
            refs[5 * n + u][...] = nm
            refs[6 * n + u][...] = nv

    specs = [_full(w.shape) for w in ws]
    shapes = [jax.ShapeDtypeStruct(w.shape, F32) for w in ws]
    out = _pcall(body, name="adamw_small", grid=(1,), in_specs=specs * 4, out_specs=specs * 3,
                 out_shape=shapes * 3)(*ws, *gs, *ms, *vs)
    return out[:n], out[n:2 * n], out[2 * n:]


def _cctx_grad(parts, c_ctx):
    def body(p_ref, c_ref, o_ref):
        acc = p_ref[0:1, :]
        for k in range(1, 4):
            acc = acc + p_ref[k:k + 1, :]
        cc = c_ref[...]
        s = _sig(cc)
        o_ref[...] = acc * (s * (1.0 + cc * (1.0 - s)))

    return _pcall(body, name="cctx_grad", grid=(1,), in_specs=[_full(parts.shape), _full((1, D))],
                  out_specs=_full((1, D)), out_shape=jax.ShapeDtypeStruct((1, D), F32))(parts, c_ctx)


ADAM_STEPS = 8


def _adamw_multi(ws, gs, ms, vs, *, name, carry=None):
    n = len(ws)

    def body(*refs):
        for u in range(n):
            refs[4 * n + u][...], refs[5 * n + u][...], refs[6 * n + u][...] = _adam_math(
                refs[u][...], refs[n + u][...], refs[2 * n + u][...], refs[3 * n + u][...])

    specs = [pl.BlockSpec((w.shape[0] // ADAM_STEPS, w.shape[1]), lambda i: (i, 0)) for w in ws]
    shapes = [jax.ShapeDtypeStruct(w.shape, F32) for w in ws]
    res = _pcall(body, name=name, grid=(ADAM_STEPS,), in_specs=specs * 4, out_specs=specs * 3,
                 out_shape=shapes * 3, carry=carry)(*ws, *gs, *ms, *vs)
    out, extra = res if carry is not None else (res, None)
    return (out[:n], out[n:2 * n], out[2 * n:]), extra


def kernel(x, c, ctx, c_ctx, w_ada, b_ada, norm_mix_w, norm_ffn_w, w_in, hgrn_lb_logits, hgrn_norm_w, q_norm_w, k_norm_w, attn_sinks, w_branch_hgrn, w_branch_attn, w_out, w_ffn_gate, w_ffn_up, w_ffn_down, loss_target, m_c_ctx, m_w_ada, m_b_ada, m_norm_mix_w, m_norm_ffn_w, m_w_in, m_hgrn_lb_logits, m_hgrn_norm_w, m_q_norm_w, m_k_norm_w, m_attn_sinks, m_w_branch_hgrn, m_w_branch_attn, m_w_out, m_w_ffn_gate, m_w_ffn_up, m_w_ffn_down, v_c_ctx, v_w_ada, v_b_ada, v_norm_mix_w, v_norm_ffn_w, v_w_in, v_hgrn_lb_logits, v_hgrn_norm_w, v_q_norm_w, v_k_norm_w, v_attn_sinks, v_w_branch_hgrn, v_w_branch_attn, v_w_out, v_w_ffn_gate, v_w_ffn_up, v_w_ffn_down):
    xi, yi, ci = _place()
    chip = 2 * xi + yi
    dev = 2 * chip + ci
    s_len = x.shape[1]

    shards = [w_in[0].T, w_branch_hgrn[0], w_branch_attn[0], w_out[0], w_ffn_gate[0].T, w_ffn_up[0].T,
              w_ffn_down[0]]
    bufs = _cast_place(shards, ci, dev)

    lbrow = jnp.pad(hgrn_lb_logits.reshape(1, 512), ((0, 0), (0, D - 512)))
    blk = jnp.concatenate([c, lbrow, jnp.zeros((6, D), F32)], axis=0)
    nada = w_ada.shape[2]
    b_sh = lax.dynamic_slice(b_ada, (0, chip * nada), (1, nada))
    g0, c16, g1, in8 = _prologue(blk, c_ctx[None], w_ada[0], b_sh, bufs[0])
    lg = g0[0::2, 1, :512].reshape(4, 2, 2, 128).transpose(1, 2, 0, 3).reshape(2, 2, HGW)
    modall = g1[0::2].transpose(1, 0, 2).reshape(16, 4 * nada)
    mod = lax.dynamic_slice(modall, (dev, 0), (1, 6 * D)).reshape(6, D)
    modc = modall[8].reshape(6, D)[:2]

    sq, gx, _, small, rs = _local_step(
        x[0], ctx[0], loss_target[0], mod, modc, norm_mix_w, norm_ffn_w, lg, hgrn_norm_w, q_norm_w,
        k_norm_w, attn_sinks[0], in8.reshape(NCOL, D), bufs[1:], dist=(ci, chip))

    def whole(r):
        return r.reshape(2 * r.shape[1], r.shape[2])

    g_dn, g_g, g_u = [whole(r) for r in rs["ffn_done"]]
    g_bh, g_ba, g_o = [whole(r) for r in rs["mix_done"]]
    in_pairs = rs["in_pairs"]
    rest_rows = (in_pairs[0].shape[1] // 2, in_pairs[0].shape[1] // 2)

    g2, tot = _ag_small(small["raw"], sq)
    loss = 0.5 * jnp.sum(tot[24]) / D
    dmodc_tot = jnp.pad(tot[6:8].reshape(1, 2 * D), ((0, 0), (0, 4 * D)))
    g_b_ada = tot[0:6].reshape(1, 6 * D) + dmodc_tot
    dmod16 = jnp.concatenate([g2[:, 0:6].reshape(8, 6 * D), dmodc_tot, jnp.zeros((7, 6 * D), F32)], axis=0)
    g_w_ada, gc_part = _ada_bwd(c16, lax.dynamic_slice(dmod16, (0, chip * nada), (16, nada)), w_ada[0])
    g3, = _allgather([gc_part[8:16]], name="ag_cctx", in_vmem=True)
    g_c_ctx = _cctx_grad(g3[0::2, 0], c_ctx[None])[0]
    g_nw1 = tot[8:9]
    g_nw2 = tot[9:10]
    g_hw = tot[10, :HGW].reshape(4, HGD).sum(0, keepdims=True)
    g_qnw = tot[10, HGW:].reshape(8, HDIM).sum(0, keepdims=True)
    g_knw = tot[11, :128].reshape(2, HDIM).sum(0, keepdims=True)
    g_sinks = tot[16:24, 0][None]
    g_lg = lax.dynamic_slice(tot[12:16, :HGW].reshape(2, 2, HGW), (0, 0, chip * 128), (2, 2, 128))

    names = ["c_ctx", "w_ada", "b_ada", "norm_mix_w", "norm_ffn_w", "w_in", "hgrn_lb_logits", "hgrn_norm_w",
             "q_norm_w", "k_norm_w", "attn_sinks", "w_branch_hgrn", "w_branch_attn", "w_out", "w_ffn_gate",
             "w_ffn_up", "w_ffn_down"]
    ws = dict(zip(names, [c_ctx, w_ada, b_ada, norm_mix_w, norm_ffn_w, w_in, hgrn_lb_logits, hgrn_norm_w,
                          q_norm_w, k_norm_w, attn_sinks, w_branch_hgrn, w_branch_attn, w_out, w_ffn_gate,
                          w_ffn_up, w_ffn_down]))
    ms = dict(zip(names, [m_c_ctx, m_w_ada, m_b_ada, m_norm_mix_w, m_norm_ffn_w, m_w_in, m_hgrn_lb_logits,
                          m_hgrn_norm_w, m_q_norm_w, m_k_norm_w, m_attn_sinks, m_w_branch_hgrn,
                          m_w_branch_attn, m_w_out, m_w_ffn_gate, m_w_ffn_up, m_w_ffn_down]))
    vs = dict(zip(names, [v_c_ctx, v_w_ada, v_b_ada, v_norm_mix_w, v_norm_ffn_w, v_w_in, v_hgrn_lb_logits,
                          v_hgrn_norm_w, v_q_norm_w, v_k_norm_w, v_attn_sinks, v_w_branch_hgrn,
                          v_w_branch_attn, v_w_out, v_w_ffn_gate, v_w_ffn_up, v_w_ffn_down]))
    transposed = ("w_in", "w_ffn_gate", "w_ffn_up")

    def view(a, n):
        return a[0].T if n in transposed else a[0]

    def unview(a, n):
        return a.T[None] if n in transposed else a[None]

    delta, new_m, new_v, grads = {}, {}, {}, {}

    def big_adamw(group, gs, name, carry=None):
        (d_, m_, v_), extra = _adamw_multi([view(ws[n], n) for n in group], gs, [view(ms[n], n) for n in group],
                                           [view(vs[n], n) for n in group], name=name, carry=carry)
        for i, n in enumerate(group):
            grads[n], delta[n], new_m[n], new_v[n] = (unview(gs[i], n), unview(d_[i], n), unview(m_[i], n),
                                                      unview(v_[i], n))
        return extra

    in_contribs = big_adamw(["w_ffn_down", "w_ffn_gate", "w_ffn_up", "w_out", "w_branch_hgrn", "w_branch_attn"],
                            [g_dn, g_g, g_u, g_o, g_bh, g_ba], "adamw_first",
                            carry=_carry_chipx(in_pairs, rows=rest_rows, into=rs["in_part"]))
    in_reds = _rs_chip_add(in_pairs, in_contribs, ci, chip)
    g_in, = [whole(r) for r in _rs_sibling_gather(in_reds)]
    big_adamw(["w_in", "w_ada"], [g_in, g_w_ada], "adamw_second")
    grads.update(c_ctx=g_c_ctx, b_ada=g_b_ada, norm_mix_w=g_nw1, norm_ffn_w=g_nw2, hgrn_lb_logits=g_lg,
                 hgrn_norm_w=g_hw, q_norm_w=g_qnw, k_norm_w=g_knw, attn_sinks=g_sinks)
    small_names = [n for n in names if n not in delta]

    def two_d(a):
        return a.reshape(1, -1) if a.ndim == 1 else a

    sd, sm_, sv = _adamw_small(*[[two_d(d[n]) for n in small_names] for d in (ws, grads, ms, vs)])
    for i, n in enumerate(small_names):
        for dst, src in ((delta, sd), (new_m, sm_), (new_v, sv)):
            dst[n] = src[i].reshape(ws[n].shape)
    return (loss, gx[None], *[grads[n] for n in names], *[delta[n] for n in names],
            *[new_m[n] for n in names], *[new_v[n] for n in names])
```

```python
import functools

import numpy as np
import jax
import jax.numpy as jnp
from jax import lax
from jax.experimental import pallas as pl
from jax.experimental.pallas import tpu as pltpu

F32 = jnp.float32
BF16 = jnp.bfloat16
HI = lax.Precision.HIGHEST
MESH = pl.DeviceIdType.MESH

D = 1024
L = 256
TM = 256
HGW = 512
HGD = 128
CH = 32
ATW = 512
HDIM = 64
BLK = 128
GRID_W = 64
DFF = 2816
NCOL = 5376
EPS = 1e-6
ROPE_THETA = 10000.0

C_FB, C_INP, C_QHG, C_FF = 0, 1, 2, 3
C_GATES = 1
C_GHG, C_QRAW = 8, 9
C_KV = 20
C_QKV = 6

ADAM_LR, ADAM_B1, ADAM_B2, ADAM_EPS, ADAM_WD, ADAM_STEP = 0.001, 0.9, 0.999, 1e-08, 0.01, 10

NN = (((1,), (0,)), ((), ()))
NT = (((1,), (1,)), ((), ()))
TN = (((0,), (0,)), ((), ()))


def _dot(a, b, dims=NN, prec=None):
    return lax.dot_general(a, b, dims, precision=prec, preferred_element_type=F32)


def _bdot(a, b, dims=NN):
    return _dot(a.astype(BF16), b.astype(BF16), dims)


def _sig(x):
    return 1.0 / (1.0 + jnp.exp(-x))


class _Carry:
    def __init__(self, ins, outs, aliases, scratch, phases):
        self.ins, self.outs, self.aliases, self.scratch, self.phases = ins, outs, aliases, scratch, phases


def _in_hbm(args):
    return [pltpu.with_memory_space_constraint(a, pltpu.HBM) for a in args]


def _out_hbm(shapes):
    if isinstance(shapes, (list, tuple)):
        return [pltpu.HBM(s.shape, s.dtype) for s in shapes]
    return pltpu.HBM(shapes.shape, shapes.dtype)


def _carry_join(a, b):
    na_in, na_out, na_sc = len(a.ins), len(a.outs), len(a.scratch)
    aliases = dict(a.aliases)
    aliases.update({na_in + i: na_out + o for i, o in b.aliases.items()})

    def phases(ins, outs, sems):
        pa = a.phases(ins[:na_in], outs[:na_out], sems[:na_sc])
        pb = b.phases(ins[na_in:], outs[na_out:], sems[na_sc:])

        def both(fa, fb):
            if fa is None and fb is None:
                return None

            def run():
                for fn in (fa, fb):
                    if fn is not None:
                        fn()
            return run

        return tuple(both(fa, fb) for fa, fb in zip(pa, pb))

    return _Carry(list(a.ins) + list(b.ins), list(a.outs) + list(b.outs), aliases,
                  list(a.scratch) + list(b.scratch), phases)


def _pcall(body, *, name, grid, in_specs, out_specs, out_shape, scratch=(), aliases=None, vmem_mb=48,
           carry=None):
    params = pltpu.CompilerParams(dimension_semantics=("arbitrary",) * len(grid),
                                  vmem_limit_bytes=vmem_mb << 20)
    if carry is None:
        plain = pl.pallas_call(
            body, name=name, grid=grid, in_specs=in_specs, out_specs=out_specs, out_shape=_out_hbm(out_shape),
            scratch_shapes=list(scratch), input_output_aliases=aliases or {}, compiler_params=params)
        return lambda *args: plain(*_in_hbm(args))
    single = not isinstance(out_shape, (list, tuple))
    out_specs_l = [out_specs] if single else list(out_specs)
    out_shape_l = [out_shape] if single else list(out_shape)
    n_in, n_out, n_sc = len(in_specs), len(out_shape_l), len(scratch)
    k_in, k_out = len(carry.ins), len(carry.outs)
    nsteps = int(np.prod(grid))
    assert nsteps >= 3

    def wrapped(*refs):
        ins, cins = refs[:n_in], refs[n_in:n_in + k_in]
        o0 = n_in + k_in
        outs, couts = refs[o0:o0 + n_out], refs[o0 + n_out:o0 + n_out + k_out]
        s0 = o0 + n_out + k_out
        sc, csc = refs[s0:s0 + n_sc], refs[s0 + n_sc:]
        step = pl.program_id(0)
        for ax in range(1, len(grid)):
            step = step * grid[ax] + pl.program_id(ax)
        start, mid, end = carry.phases(cins, couts, csc)
        pl.when(step == 0)(start)
        body(*ins, *outs, *sc)
        if mid is not None:
            pl.when(step == nsteps - 2)(mid)
        pl.when(step == nsteps - 1)(end)

    all_aliases = dict(aliases or {})
    all_aliases.update({n_in + i: n_out + o for i, o in carry.aliases.items()})
    call = pl.pallas_call(
        wrapped, name=name, grid=grid, in_specs=list(in_specs) + [ANY] * k_in,
        out_specs=out_specs_l + [ANY] * k_out, out_shape=_out_hbm(out_shape_l + list(carry.outs)),
        scratch_shapes=list(scratch) + list(carry.scratch), input_output_aliases=all_aliases,
        compiler_params=params)

    def run(*args):
        res = call(*_in_hbm(args), *carry.ins)
        core = res[:n_out]
        return (core[0] if single else list(core)), list(res[n_out:])

    return run


def _full(shape):
    nd = len(shape)
    return pl.BlockSpec(shape, lambda *_: (0,) * nd)


ANY = pl.BlockSpec(memory_space=pl.ANY)


def _mm(a, b, *, name, mode="nn", out_dtype=F32, tm, tn, tk):
    if mode == "nn":
        (m, k), (k2, n) = a.shape, b.shape
    elif mode == "nt":
        (m, k), (n, k2) = a.shape, b.shape
    else:
        (k, m), (k2, n) = a.shape, b.shape
    assert k == k2 and m % tm == 0 and n % tn == 0 and k % tk == 0, (name, a.shape, b.shape)
    nk = k // tk
    dims = {"nn": NN, "nt": NT, "tn": TN}[mode]

    def body(a_ref, b_ref, o_ref, acc):
        kk = pl.program_id(2)

        @pl.when(kk == 0)
        def _():
            acc[...] = jnp.zeros_like(acc)

        acc[...] += _bdot(a_ref[...], b_ref[...], dims)

        @pl.when(kk == nk - 1)
        def _():
            o_ref[...] = acc[...].astype(out_dtype)

    a_spec = (pl.BlockSpec((tk, tm), lambda i, j, kk: (kk, i)) if mode == "tn"
              else pl.BlockSpec((tm, tk), lambda i, j, kk: (i, kk)))
    b_spec = (pl.BlockSpec((tn, tk), lambda i, j, kk: (j, kk)) if mode == "nt"
              else pl.BlockSpec((tk, tn), lambda i, j, kk: (kk, j)))
    return _pcall(body, name=name, grid=(m // tm, n // tn, nk), in_specs=[a_spec, b_spec],
                  out_specs=pl.BlockSpec((tm, tn), lambda i, j, kk: (i, j)),
                  out_shape=jax.ShapeDtypeStruct((m, n), out_dtype),
                  scratch=[pltpu.VMEM((tm, tn), F32)])(a, b)


NT_IN = NCOL // 256


def _src_block(j):
    return j + jnp.where(j < 4, 2, jnp.where(j < 6, 3, jnp.where(j < 8, -6, jnp.where(
        j < 16, 5, jnp.where(j < 20, -7, -14)))))


def _mm_in(h, wt, tm, carry=None):
    tt = h.shape[0]

    def body(h_ref, w_ref, o_ref):
        o_ref[...] = _bdot(h_ref[...], w_ref[...], NT)

    return _pcall(body, name="mm_in", grid=(tt // tm, NT_IN),
                  in_specs=[pl.BlockSpec((tm, D), lambda i, j: (i, 0)),
                            pl.BlockSpec((256, D), lambda i, j: (_src_block(j), 0))],
                  out_specs=pl.BlockSpec((tm, 256), lambda i, j: (i, j)),
                  out_shape=jax.ShapeDtypeStruct((tt, NCOL), F32), carry=carry)(h, wt)


def _mm_dh(dp, wt, tm, carry=None):
    tt = dp.shape[0]
    per, ng = 3, NT_IN // 3

    def body(d_ref, w0, w1, w2, o_ref, acc):
        kk = pl.program_id(1)

        @pl.when(kk == 0)
        def _():
            acc[...] = jnp.zeros_like(acc)

        acc[...] += (_bdot(d_ref[:, 0:256], w0[...]) + _bdot(d_ref[:, 256:512], w1[...])
                     + _bdot(d_ref[:, 512:768], w2[...]))

        @pl.when(kk == ng - 1)
        def _():
            o_ref[...] = acc[...]

    wspecs = [pl.BlockSpec((256, D), functools.partial(lambda t, i, kk: (_src_block(per * kk + t), 0), t))
              for t in range(per)]
    return _pcall(body, name="mm_dh", grid=(tt // tm, ng),
                  in_specs=[pl.BlockSpec((tm, per * 256), lambda i, kk: (i, kk))] + wspecs,
                  out_specs=pl.BlockSpec((tm, D), lambda i, kk: (i, 0)),
                  out_shape=jax.ShapeDtypeStruct((tt, D), F32), scratch=[pltpu.VMEM((tm, D), F32)],
                  carry=carry)(dp, wt, wt, wt)


def _mm_gin(dp, h, tk):
    tt = dp.shape[0]
    nk = tt // tk

    def body(d_ref, h_ref, o_ref, acc):
        kk = pl.program_id(1)

        @pl.when(kk == 0)
        def _():
            acc[...] = jnp.zeros_like(acc)

        acc[...] += _bdot(d_ref[...], h_ref[...], TN)

        @pl.when(kk == nk - 1)
        def _():
            o_ref[...] = acc[...].astype(BF16)

    return _pcall(body, name="mm_gin", grid=(NT_IN, nk),
                  in_specs=[pl.BlockSpec((tk, 256), lambda j, kk: (kk, j)),
                            pl.BlockSpec((tk, D), lambda j, kk: (kk, 0))],
                  out_specs=pl.BlockSpec((256, D), lambda j, kk: (_src_block(j), 0)),
                  out_shape=jax.ShapeDtypeStruct((NCOL, D), BF16), scratch=[pltpu.VMEM((256, D), F32)])(dp, h)


def _tok_specs():
    assert L == TM
    return [_full((TM, D)), pl.BlockSpec((TM, D), lambda i: (jnp.maximum(i - 1, 0), 0))]


def _mod1(ctx, x, nw, ss):
    rows = L + x.shape[0]

    def body(c_ref, x_ref, nw_ref, ss_ref, h_ref):
        t = jnp.where(pl.program_id(0) == 0, c_ref[...], x_ref[...])
        r = lax.rsqrt(jnp.mean(t * t, axis=-1, keepdims=True) + EPS)
        s = ss_ref[0]
        h_ref[...] = ((t * r * nw_ref[...]) * (1.0 + s[1:2]) + s[0:1]).astype(BF16)

    return _pcall(body, name="mod1", grid=(rows // TM,),
                  in_specs=_tok_specs() + [_full((1, D)),
                                           pl.BlockSpec((1, 2, D), lambda i: (jnp.minimum(i, 1), 0, 0))],
                  out_specs=pl.BlockSpec((TM, D), lambda i: (i, 0)),
                  out_shape=jax.ShapeDtypeStruct((rows, D), BF16))(ctx, x, nw, ss)


def _norm_bwd_rows(x, dh, nw, scale):
    r = lax.rsqrt(jnp.mean(x * x, axis=-1, keepdims=True) + EPS)
    xh = x * r
    dxh = dh * ((1.0 + scale) * nw)
    dx = r * (dxh - xh * jnp.mean(dxh * xh, axis=-1, keepdims=True))
    return dx, xh


def _out_proj_mod2(mixed, w_o, x, g1, nw2, ss2):
    s_len = x.shape[0]
    tm = 512

    def body(m_ref, w_ref, x_ref, g_ref, nw_ref, ss_ref, ao_ref, x1_ref, h_ref):
        ao = _bdot(m_ref[...], w_ref[...])
        ao_ref[...] = ao
        x1 = x_ref[...] + g_ref[...] * ao
        x1_ref[...] = x1
        r = lax.rsqrt(jnp.mean(x1 * x1, axis=-1, keepdims=True) + EPS)
        s = ss_ref[0]
        h_ref[...] = ((x1 * r * nw_ref[...]) * (1.0 + s[1:2]) + s[0:1]).astype(BF16)

    row = pl.BlockSpec((tm, D), lambda i: (i, 0))
    f = jax.ShapeDtypeStruct((s_len, D), F32)
    return _pcall(body, name="out_proj_mod2", grid=(s_len // tm,),
                  in_specs=[row, _full((D, D)), row, _full((1, D)), _full((1, D)), _full((1, 2, D))],
                  out_specs=[row, row, row],
                  out_shape=[f, f, jax.ShapeDtypeStruct((s_len, D), BF16)])(mixed, w_o, x, g1, nw2, ss2)


TS = 1024


def _acc_call(body, *, name, grid, in_specs, out_specs, out_shape, acc_shapes, args, carry=None):
    return _pcall(body, name=name, grid=grid, in_specs=in_specs, out_specs=out_specs, out_shape=out_shape,
                  scratch=[pltpu.VMEM(s, F32) for s in acc_shapes], carry=carry)(*args)


def _mm_cs(a, w4, *, name):
    m, k = a.shape
    _, _, ns = w4.shape

    def body(a_ref, w_ref, o_ref):
        o_ref[...] = _bdot(a_ref[...], w_ref[0])

    return _pcall(body, name=name, grid=(m // TS, 4),
                  in_specs=[pl.BlockSpec((TS, k), lambda i, j: (i, 0)),
                            pl.BlockSpec((1, k, ns), lambda i, j: (j, 0, 0))],
                  out_specs=pl.BlockSpec((TS, ns), lambda i, j: (i, j)),
                  out_shape=jax.ShapeDtypeStruct((m, 4 * ns), F32))(a, w4)


def _mm_cs_nt(a, w4, *, name):
    m = a.shape[0]
    _, k, ns = w4.shape

    def body(a_ref, w_ref, o_ref, acc):
        j = pl.program_id(1)

        @pl.when(j == 0)
        def _():
            acc[...] = jnp.zeros_like(acc)

        acc[...] += _bdot(a_ref[...], w_ref[0], NT)

        @pl.when(j == 3)
        def _():
            o_ref[...] = acc[...]

    return _acc_call(body, name=name, grid=(m // TS, 4),
                     in_specs=[pl.BlockSpec((TS, ns), lambda i, j: (i, j)),
                               pl.BlockSpec((1, k, ns), lambda i, j: (j, 0, 0))],
                     out_specs=pl.BlockSpec((TS, k), lambda i, j: (i, 0)),
                     out_shape=jax.ShapeDtypeStruct((m, k), F32), acc_shapes=[(TS, k)], args=(a, w4))


def _mm_cs_tn(a, b, ns, *, name):
    s_len, k = a.shape
    nk = s_len // TS

    def body(a_ref, b_ref, o_ref, acc):
        t = pl.program_id(1)

        @pl.when(t == 0)
        def _():
            acc[...] = jnp.zeros_like(acc)

        acc[...] += _bdot(a_ref[...], b_ref[...], TN)

        @pl.when(t == nk - 1)
        def _():
            o_ref[0] = acc[...].astype(o_ref.dtype)

    return _acc_call(body, name=name, grid=(4, nk),
                     in_specs=[pl.BlockSpec((TS, k), lambda j, t: (t, 0)),
                               pl.BlockSpec((TS, ns), lambda j, t: (t, j))],
                     out_specs=pl.BlockSpec((1, k, ns), lambda j, t: (j, 0, 0)),
                     out_shape=jax.ShapeDtypeStruct((4, k, ns), BF16), acc_shapes=[(k, ns)], args=(a, b))


def _ffn_up(h2, g4, u4, carry=None):
    s_len = h2.shape[0]
    ns = g4.shape[1]

    def body(h_ref, g_ref, u_ref, a_ref, b_ref, z_ref):
        h = h_ref[...]
        a = _bdot(h, g_ref[0], NT)
        b = _bdot(h, u_ref[0], NT)
        a_ref[0] = a.astype(BF16)
        b_ref[0] = b.astype(BF16)
        z_ref[0] = (a * _sig(a) * b).astype(BF16)

    w = pl.BlockSpec((1, ns, D), lambda i, j: (j, 0, 0))
    o = pl.BlockSpec((1, TS, ns), lambda i, j: (j, i, 0))
    f = jax.ShapeDtypeStruct((4, s_len, ns), BF16)
    return _pcall(body, name="ffn_up", grid=(s_len // TS, 4),
                  in_specs=[pl.BlockSpec((TS, D), lambda i, j: (i, 0)), w, w], out_specs=[o, o, o],
                  out_shape=[f, f, jax.ShapeDtypeStruct((4, s_len, ns), BF16)], carry=carry)(h2, g4, u4)


def _ffn_down_loss(z4, dn4, x1, g2, tgt):
    _, s_len, ns = z4.shape

    def body(z_ref, w_ref, x1_ref, g_ref, t_ref, sq_ref, dx2_ref, dyb_ref, dg_ref, acc):
        i, j = pl.program_id(0), pl.program_id(1)

        @pl.when((i == 0) & (j == 0))
        def _():
            sq_ref[...] = jnp.zeros_like(sq_ref)
            dg_ref[...] = jnp.zeros_like(dg_ref)

        @pl.when(j == 0)
        def _():
            acc[...] = jnp.zeros_like(acc)

        acc[...] += _bdot(z_ref[0], w_ref[0])

        @pl.when(j == 3)
        def _():
            y_ = acc[...]
            g = g_ref[...]
            e = x1_ref[...] + g * y_ - t_ref[...]
            sq_ref[...] += jnp.sum(e * e, axis=0, keepdims=True)
            dx2 = e * (1.0 / D)
            dx2_ref[...] = dx2
            dyb_ref[...] = (g * dx2).astype(BF16)
            dg_ref[...] += jnp.sum(dx2 * y_, axis=0, keepdims=True)

    row = pl.BlockSpec((TS, D), lambda i, j: (i, 0))
    vec = _full((1, D))
    return _acc_call(body, name="ffn_down_loss", grid=(s_len // TS, 4),
                     in_specs=[pl.BlockSpec((1, TS, ns), lambda i, j: (j, i, 0)),
                               pl.BlockSpec((1, ns, D), lambda i, j: (j, 0, 0)), row, vec, row],
                     out_specs=[vec, row, row, vec],
                     out_shape=[jax.ShapeDtypeStruct((1, D), F32), jax.ShapeDtypeStruct((s_len, D), F32),
                                jax.ShapeDtypeStruct((s_len, D), BF16), jax.ShapeDtypeStruct((1, D), F32)],
                     acc_shapes=[(TS, D)], args=(z4, dn4, x1, g2, tgt))


def _ffn_dz(dyb, dn4, a4, b4):
    _, s_len, ns = a4.shape

    def body(dy_ref, w_ref, a_ref, b_ref, da_ref, db_ref):
        dz = _bdot(dy_ref[...], w_ref[0], NT)
        a = a_ref[0].astype(F32)
        s = _sig(a)
        da_ref[0] = (dz * b_ref[0].astype(F32) * (s * (1.0 + a * (1.0 - s)))).astype(BF16)
        db_ref[0] = (dz * (a * s)).astype(BF16)

    t = pl.BlockSpec((1, TS, ns), lambda i, j: (j, i, 0))
    o = jax.ShapeDtypeStruct((4, s_len, ns), BF16)
    return _pcall(body, name="ffn_dz", grid=(s_len // TS, 4),
                  in_specs=[pl.BlockSpec((TS, D), lambda i, j: (i, 0)),
                            pl.BlockSpec((1, ns, D), lambda i, j: (j, 0, 0)), t, t],
                  out_specs=[t, t], out_shape=[o, o])(dyb, dn4, a4, b4)


def _ffn_gdn(z4, dyb):
    _, s_len, ns = z4.shape
    nk = s_len // TS

    def body(z_ref, dy_ref, o_ref, acc):
        t = pl.program_id(1)

        @pl.when(t == 0)
        def _():
            acc[...] = jnp.zeros_like(acc)

        acc[...] += _bdot(z_ref[0], dy_ref[...], TN)

        @pl.when(t == nk - 1)
        def _():
            o_ref[0] = acc[...].astype(o_ref.dtype)

    return _acc_call(body, name="ffn_gdn", grid=(4, nk),
                     in_specs=[pl.BlockSpec((1, TS, ns), lambda j, t: (j, t, 0)),
                               pl.BlockSpec((TS, D), lambda j, t: (t, 0))],
                     out_specs=pl.BlockSpec((1, ns, D), lambda j, t: (j, 0, 0)),
                     out_shape=jax.ShapeDtypeStruct((4, ns, D), BF16), acc_shapes=[(ns, D)], args=(z4, dyb))


def _ffn_dh2(da4, db4, g4, u4, carry=None):
    _, s_len, ns = da4.shape

    def body(da_ref, db_ref, g_ref, u_ref, o_ref, acc):
        j = pl.program_id(1)

        @pl.when(j == 0)
        def _():
            acc[...] = jnp.zeros_like(acc)

        acc[...] += _bdot(da_ref[0], g_ref[0]) + _bdot(db_ref[0], u_ref[0])

        @pl.when(j == 3)
        def _():
            o_ref[...] = acc[...]

    t = pl.BlockSpec((1, TS, ns), lambda i, j: (j, i, 0))
    w = pl.BlockSpec((1, ns, D), lambda i, j: (j, 0, 0))
    return _acc_call(body, name="ffn_dh2", grid=(s_len // TS, 4), in_specs=[t, t, w, w],
                     out_specs=pl.BlockSpec((TS, D), lambda i, j: (i, 0)),
                     out_shape=jax.ShapeDtypeStruct((s_len, D), F32), acc_shapes=[(TS, D)],
                     args=(da4, db4, g4, u4), carry=carry)


def _ffn_ggu(h2, da4, db4, carry=None):
    _, s_len, ns = da4.shape
    nk = s_len // TS

    def body(h_ref, da_ref, db_ref, gg_ref, gu_ref, acc_g, acc_u):
        t = pl.program_id(1)

        @pl.when(t == 0)
        def _():
            acc_g[...] = jnp.zeros_like(acc_g)
            acc_u[...] = jnp.zeros_like(acc_u)

        h = h_ref[...]
        acc_g[...] += _bdot(da_ref[0], h, TN)
        acc_u[...] += _bdot(db_ref[0], h, TN)

        @pl.when(t == nk - 1)
        def _():
            gg_ref[0] = acc_g[...].astype(BF16)
            gu_ref[0] = acc_u[...].astype(BF16)

    d = pl.BlockSpec((1, TS, ns), lambda j, t: (j, t, 0))
    o = pl.BlockSpec((1, ns, D), lambda j, t: (j, 0, 0))
    f = jax.ShapeDtypeStruct((4, ns, D), BF16)
    return _acc_call(body, name="ffn_ggu", grid=(4, nk),
                     in_specs=[pl.BlockSpec((TS, D), lambda j, t: (t, 0)), d, d], out_specs=[o, o],
                     out_shape=[f, f], acc_shapes=[(ns, D), (ns, D)], args=(h2, da4, db4), carry=carry)


def _mod2_bwd(x1, dh2, dx2, ao, nw2, ss2, g1):
    s_len = x1.shape[0]

    def body(x1_ref, dh_ref, dx2_ref, ao_ref, nw_ref, ss_ref, g_ref,
             dx1_ref, da_ref, dss_ref, dnw_ref, dg_ref):
        i = pl.program_id(0)

        @pl.when(i == 0)
        def _():
            dss_ref[...] = jnp.zeros_like(dss_ref)
            dnw_ref[...] = jnp.zeros_like(dnw_ref)
            dg_ref[...] = jnp.zeros_like(dg_ref)

        dh = dh_ref[...]
        nw = nw_ref[...]
        scale = ss_ref[0][1:2]
        dxn, xh = _norm_bwd_rows(x1_ref[...], dh, nw, scale)
        dx1 = dx2_ref[...] + dxn
        dx1_ref[...] = dx1
        da_ref[...] = (g_ref[...] * dx1).astype(BF16)
        dg_ref[...] += jnp.sum(dx1 * ao_ref[...], axis=0, keepdims=True)
        dsh = jnp.sum(dh, axis=0, keepdims=True)
        dsc = jnp.sum(dh * xh * nw, axis=0, keepdims=True)
        dss_ref[...] += jnp.concatenate([dsh, dsc], axis=0)
        dnw_ref[...] += jnp.sum(dh * xh * (1.0 + scale), axis=0, keepdims=True)

    row = pl.BlockSpec((TM, D), lambda i: (i, 0))
    vec = _full((1, D))
    return _pcall(body, name="mod2_bwd", grid=(s_len // TM,),
                  in_specs=[row, row, row, row, vec, _full((1, 2, D)), vec],
                  out_specs=[row, row, _full((2, D)), vec, vec],
                  out_shape=[jax.ShapeDtypeStruct((s_len, D), F32), jax.ShapeDtypeStruct((s_len, D), BF16),
                             jax.ShapeDtypeStruct((2, D), F32), jax.ShapeDtypeStruct((1, D), F32),
                             jax.ShapeDtypeStruct((1, D), F32)])(x1, dh2, dx2, ao, nw2, ss2, g1)


def _mod1_bwd(ctx, x, dh, dx1, nw1, ss1, carry=None):
    s_len = dx1.shape[0]
    tt = L + s_len

    def body(c_ref, x_ref, dh_ref, dx1_ref, nw_ref, ss_ref, dx_ref, dss_ref, dnw_ref):
        i = pl.program_id(0)
        tok = jnp.where(i == 0, c_ref[...], x_ref[...])

        @pl.when(i == 0)
        def _():
            dnw_ref[...] = jnp.zeros_like(dnw_ref)

        @pl.when(i <= 1)
        def _():
            dss_ref[...] = jnp.zeros_like(dss_ref)

        dh_ = dh_ref[...]
        nw = nw_ref[...]
        scale = ss_ref[0][1:2]
        dxn, xh = _norm_bwd_rows(tok, dh_, nw, scale)

        @pl.when(i >= 1)
        def _():
            dx_ref[...] = dx1_ref[...] + dxn

        dsh = jnp.sum(dh_, axis=0, keepdims=True)
        dsc = jnp.sum(dh_ * xh * nw, axis=0, keepdims=True)
        dss_ref[...] += jnp.concatenate([dsh, dsc], axis=0)[None]
        dnw_ref[...] += jnp.sum(dh_ * xh * (1.0 + scale), axis=0, keepdims=True)

    row = pl.BlockSpec((TM, D), lambda i: (i, 0))
    lat = pl.BlockSpec((TM, D), lambda i: (jnp.maximum(i - 1, 0), 0))
    sel = pl.BlockSpec((1, 2, D), lambda i: (jnp.minimum(i, 1), 0, 0))
    return _pcall(body, name="mod1_bwd", grid=(tt // TM,),
                  in_specs=_tok_specs() + [row, lat, _full((1, D)), sel],
                  out_specs=[lat, sel, _full((1, D))],
                  out_shape=[jax.ShapeDtypeStruct((s_len, D), F32), jax.ShapeDtypeStruct((2, 2, D), F32),
                             jax.ShapeDtypeStruct((1, D), F32)], carry=carry)(ctx, x, dh, dx1, nw1, ss1)


def _rows(c):
    return slice(c * CH, (c + 1) * CH)


def _chunk_masks(rev, transpose=False):
    r = lax.broadcasted_iota(jnp.int32, (TM, TM), 0)
    c = lax.broadcasted_iota(jnp.int32, (TM, TM), 1)
    same = (r // CH) == (c // CH)
    before = (c >= r) if (rev != transpose) else (c <= r)
    return same & before, same


def _chunk_scan(x, rev, transpose=False):
    r = lax.broadcasted_iota(jnp.int32, (CH, CH), 0)
    c = lax.broadcasted_iota(jnp.int32, (CH, CH), 1)
    tri = ((c >= r) if (rev != transpose) else (c <= r)).astype(F32)
    return jnp.concatenate([_dot(tri, x[_rows(ch)], prec=HI) for ch in range(x.shape[0] // CH)], axis=0)


def _chunk_total(x):
    return jnp.concatenate([jnp.broadcast_to(jnp.sum(x[_rows(ch)], axis=0, keepdims=True), (CH, x.shape[1]))
                            for ch in range(x.shape[0] // CH)], axis=0)


def _hgrn_gate(fl, qraw, lg):
    lb = 1.0 / (1.0 + jnp.exp(lg[1:2] - lg[0:1]))
    sg = _sig(fl)
    f = lb + (1.0 - lb) * sg
    q = qraw * _sig(qraw) * (HGD ** -0.5)
    return lb, sg, f, q


def _hgrn_fwd(p, lg, *, rev, carry=None, readout=None):
    tt = p.shape[0]
    nt = tt // TM
    ncht = TM // CH
    d = 1 if rev else 0

    def tile_of(s):
        return jnp.where(s == 0, 0, nt - s) if rev else s

    def body(*refs):
        if readout is None:
            f_ref, inp_ref, q_ref, lg_ref, o_ref, st_ref, state = refs
        else:
            f_ref, inp_ref, q_ref, lg_ref, oo_ref, g_ref, hw_ref, o_ref, st_ref, y_ref, state = refs
        s = pl.program_id(0)

        @pl.when(s == 0)
        def _():
            state[...] = jnp.zeros_like(state)

        _, _, f, q = _hgrn_gate(f_ref[...], q_ref[...], lg_ref[0])
        lf = jnp.log(f)
        causal, _ = _chunk_masks(rev)
        cum = _chunk_scan(lf, rev)
        tot = _chunk_total(lf)
        qd = (q * jnp.exp(cum)).astype(BF16)
        kd = ((1.0 - f) * jnp.exp(-cum)).astype(BF16)
        ke = ((1.0 - f) * jnp.exp(tot - cum)).astype(BF16)
        et = jnp.exp(tot)
        v = inp_ref[...].astype(BF16)
        order = range(ncht - 1, -1, -1) if rev else range(ncht)
        outs = []
        for h in range(4):
            sl = slice(h * HGD, (h + 1) * HGD)
            qd_, kd_, ke_, v_ = qd[:, sl], kd[:, sl], ke[:, sl], v[:, sl]
            pm = jnp.where(causal, _dot(qd_, kd_, NT), 0.0).astype(BF16)
            o_h = _dot(pm, v_)
            upd = [_dot(v_[_rows(c)], ke_[_rows(c)], TN) for c in range(ncht)]
            st = state[h]
            for c in order:
                st_ref[c, h] = st
                st = st * et[c * CH:c * CH + 1, sl] + upd[c]
            state[h] = st
            inter = [_dot(qd_[_rows(c)], st_ref[c, h].astype(BF16), NT) for c in range(ncht)]
            outs.append(o_h + jnp.concatenate(inter, axis=0))
        o_tile = jnp.concatenate(outs, axis=1)
        o_ref[...] = o_tile
        if readout is not None:
            @pl.when(tile_of(s) >= 1)
            def _():
                g = g_ref[...]
                y_ref[...] = (_head_rms(oo_ref[...] + o_tile, None, 4) * hw_ref[...] * (g * _sig(g))).astype(BF16)

    def col(cb):
        return pl.BlockSpec((TM, HGW), lambda s: (tile_of(s), cb))

    in_specs = [col(C_FB if rev else C_FF), col(C_INP), col(C_QHG), pl.BlockSpec((1, 2, HGW), lambda s: (d, 0, 0))]
    out_specs = [col(0), pl.BlockSpec((ncht, 4, HGD, HGD), lambda s: (tile_of(s), 0, 0, 0))]
    out_shape = [jax.ShapeDtypeStruct((tt, HGW), F32), jax.ShapeDtypeStruct((nt * ncht, 4, HGD, HGD), F32)]
    args = [p, p, p, lg]
    if readout is not None:
        in_specs += [col(0), col(C_GHG), _full((1, HGW))]
        args += [readout[0], p, readout[1]]
        assert rev
        out_specs.append(pl.BlockSpec((TM, HGW), lambda s: (jnp.where(s == 0, nt - 2, tile_of(s) - 1), 0)))
        out_shape.append(jax.ShapeDtypeStruct((tt - L, HGW), BF16))
    return _pcall(body, name="hgrn_fwd_rev" if rev else "hgrn_fwd", grid=(nt,), in_specs=in_specs,
                  out_specs=out_specs, out_shape=out_shape, scratch=[pltpu.VMEM((4, HGD, HGD), F32)],
                  carry=carry)(*args)


def _hgrn_bwd(p, lg, do, st, dp, prev, *, rev, carry=None):
    tt = p.shape[0]
    nt = tt // TM
    ncht = TM // CH
    d = 1 if rev else 0
    second = prev is not None

    def tile_of(s):
        return jnp.where(s == nt - 1, 0, s + 1) if rev else nt - 1 - s

    def body(*refs):
        if second:
            (f_ref, inp_ref, q_ref, lg_ref, do_ref, st_ref, dvp_ref, dqp_ref, _dp_in,
             dp_ref, dlg_ref, dstate) = refs
        else:
            (f_ref, inp_ref, q_ref, lg_ref, do_ref, st_ref, _dp_in,
             dp_ref, dv_ref, dq_ref, dlg_ref, dstate) = refs
        s = pl.program_id(0)
        tile = tile_of(s)

        @pl.when(s == 0)
        def _():
            dstate[...] = jnp.zeros_like(dstate)
            dlg_ref[...] = jnp.zeros_like(dlg_ref)

        qraw = q_ref[...]
        lb, sg, f, q = _hgrn_gate(f_ref[...], qraw, lg_ref[0])
        lf = jnp.log(f)
        causal, _ = _chunk_masks(rev)
        causal_t, _ = _chunk_masks(rev, transpose=True)
        cum = _chunk_scan(lf, rev)
        tot = _chunk_total(lf)
        ea, eb, ee, et = jnp.exp(cum), jnp.exp(-cum), jnp.exp(tot - cum), jnp.exp(tot)
        qdf, kdf, kef = q * ea, (1.0 - f) * eb, (1.0 - f) * ee
        qd, kd, ke = qdf.astype(BF16), kdf.astype(BF16), kef.astype(BF16)
        v = inp_ref[...].astype(BF16)
        dob = jnp.where(tile == 0, 0.0, do_ref[...]).astype(BF16)
        order = range(ncht) if rev else range(ncht - 1, -1, -1)
        dq_l, dk_l, dv_l, dcum_l, dtot_l = [], [], [], [], []
        for h in range(4):
            sl = slice(h * HGD, (h + 1) * HGD)
            qd_, kd_, ke_, v_, do_ = qd[:, sl], kd[:, sl], ke[:, sl], v[:, sl], dob[:, sl]
            pmt = jnp.where(causal_t, _dot(kd_, qd_, NT), 0.0).astype(BF16)
            dpm = jnp.where(causal, _dot(do_, v_, NT), 0.0).astype(BF16)
            dpmt = jnp.where(causal_t, _dot(v_, do_, NT), 0.0).astype(BF16)
            dv = _dot(pmt, do_)
            dqd = _dot(dpm, kd_)
            dkd = _dot(dpmt, qd_)
            upd = [_dot(do_[_rows(c)], qd_[_rows(c)], TN) for c in range(ncht)]
            ds = dstate[h]
            ds1 = [None] * ncht
            for c in order:
                ds1[c] = ds
                ds = ds * et[c * CH:c * CH + 1, sl] + upd[c]
            dstate[h] = ds
            dke_c, dv_c, dqd_c, dtot_c = [], [], [], []
            for c in range(ncht):
                st0 = st_ref[c, h]
                dsb = ds1[c].astype(BF16)
                dke_ = _dot(v_[_rows(c)], dsb)
                dke_c.append(dke_)
                dv_c.append(_dot(ke_[_rows(c)], dsb, NT))
                dqd_c.append(_dot(do_[_rows(c)], st0.astype(BF16)))
                dt = (jnp.sum(ds1[c] * st0, axis=0, keepdims=True) * et[c * CH:c * CH + 1, sl]
                      + jnp.sum(dke_ * kef[_rows(c), sl], axis=0, keepdims=True))
                dtot_c.append(jnp.broadcast_to(dt, (CH, HGD)))
            dke = jnp.concatenate(dke_c, axis=0)
            dqd = dqd + jnp.concatenate(dqd_c, axis=0)
            dv_l.append(dv + jnp.concatenate(dv_c, axis=0))
            dtot_l.append(jnp.concatenate(dtot_c, axis=0))
            dq_l.append(dqd * ea[:, sl])
            dk_l.append(dkd * eb[:, sl] + dke * ee[:, sl])
            dcum_l.append(dqd * qdf[:, sl] - dkd * kdf[:, sl] - dke * kef[:, sl])
        dcum = jnp.concatenate(dcum_l, axis=1)
        dlf = _chunk_scan(dcum, rev, transpose=True) + jnp.concatenate(dtot_l, axis=1)
        dq_t = jnp.concatenate(dq_l, axis=1)
        dv_t = jnp.concatenate(dv_l, axis=1)

        df = dlf / f - jnp.concatenate(dk_l, axis=1)
        dfl = df * (1.0 - lb) * sg * (1.0 - sg)
        dlb = jnp.sum(df * (1.0 - sg), axis=0, keepdims=True)
        dl0 = dlb * lb * (1.0 - lb)
        dlg_ref[...] += jnp.concatenate([dl0, -dl0], axis=0)[None]
        if second:
            sq = _sig(qraw)
            dqr = (dqp_ref[...] + dq_t) * (HGD ** -0.5) * (sq * (1.0 + qraw * (1.0 - sq)))
            dp_ref[...] = jnp.concatenate([dfl, dvp_ref[...] + dv_t, dqr], axis=1).astype(BF16)
        else:
            dp_ref[...] = dfl.astype(BF16)
            dv_ref[...] = dv_t
            dq_ref[...] = dq_t

    def col(cb):
        return pl.BlockSpec((TM, HGW), lambda s: (tile_of(s), cb))

    tok = pl.BlockSpec((TM, HGW), lambda s: (tile_of(s), 0))
    in_specs = [col(C_FB if rev else C_FF), col(C_INP), col(C_QHG),
                pl.BlockSpec((1, 2, HGW), lambda s: (d, 0, 0)),
                pl.BlockSpec((TM, HGW), lambda s: (jnp.maximum(tile_of(s) - 1, 0), 0)),
                pl.BlockSpec((ncht, 4, HGD, HGD), lambda s: (tile_of(s), 0, 0, 0))]
    args = [p, p, p, lg, do, st]
    dlg_spec = _full((1, 2, HGW))
    dlg_shape = jax.ShapeDtypeStruct((1, 2, HGW), F32)
    if second:
        in_specs += [tok, tok]
        args += [prev[0], prev[1]]
        out_specs = [pl.BlockSpec((TM, 3 * HGW), lambda s: (tile_of(s), 0)), dlg_spec]
        out_shape = [jax.ShapeDtypeStruct(dp.shape, BF16), dlg_shape]
    else:
        out_specs = [pl.BlockSpec((TM, HGW), lambda s: (tile_of(s), C_FB if rev else C_FF)), tok, tok, dlg_spec]
        out_shape = [jax.ShapeDtypeStruct(dp.shape, BF16), jax.ShapeDtypeStruct((tt, HGW), F32),
                     jax.ShapeDtypeStruct((tt, HGW), F32), dlg_shape]
    in_specs.append(ANY)
    args.append(dp)
    return _pcall(body, name="hgrn_bwd_rev" if rev else "hgrn_bwd", grid=(nt,),
                  in_specs=in_specs, out_specs=out_specs, out_shape=out_shape,
                  scratch=[pltpu.VMEM((4, HGD, HGD), F32)],
                  aliases={len(args) - 1: 0}, carry=carry)(*args)


def _head_rms(o, w, nheads):
    outs = []
    for h in range(nheads):
        oh = o[:, h * HGD:(h + 1) * HGD]
        outs.append(oh * lax.rsqrt(jnp.mean(oh * oh, axis=-1, keepdims=True) + EPS))
    return jnp.concatenate(outs, axis=1)


def _readout(o0, o1, p, hw4):
    s_len = o0.shape[0] - L

    def body(o0_ref, o1_ref, g_ref, w_ref, y_ref):
        xh = _head_rms(o0_ref[...] + o1_ref[...], None, 4)
        g = g_ref[...]
        y_ref[...] = (xh * w_ref[...] * (g * _sig(g))).astype(BF16)

    lat = pl.BlockSpec((TM, HGW), lambda i: (i + 1, 0))
    return _pcall(body, name="readout", grid=(s_len // TM,),
                  in_specs=[lat, lat, pl.BlockSpec((TM, HGW), lambda i: (i + 1, C_GHG)), _full((1, HGW))],
                  out_specs=pl.BlockSpec((TM, HGW), lambda i: (i, 0)),
                  out_shape=jax.ShapeDtypeStruct((s_len, HGW), BF16))(o0, o1, p, hw4)


def _readout_bwd(o0, o1, p, hw4, dy, dp, carry=None):
    tt = o0.shape[0]
    s_len = tt - L

    def body(o0_ref, o1_ref, g_ref, w_ref, dy_ref, _dp_in, dp_ref, do_ref, dw_ref):
        i = pl.program_id(0)

        @pl.when(i == 0)
        def _():
            dw_ref[...] = jnp.zeros_like(dw_ref)
            dp_ref[...] = jnp.zeros_like(dp_ref)

        @pl.when(i >= 1)
        def _():
            o = o0_ref[...] + o1_ref[...]
            g = g_ref[...]
            w = w_ref[...]
            sg = _sig(g)
            dy_ = dy_ref[...]
            dsw = dy_ * (g * sg)
            outs, xhs = [], []
            for h in range(4):
                sl = slice(h * HGD, (h + 1) * HGD)
                oh = o[:, sl]
                r = lax.rsqrt(jnp.mean(oh * oh, axis=-1, keepdims=True) + EPS)
                xh = oh * r
                dxh = dsw[:, sl] * w[:, sl]
                outs.append(r * (dxh - xh * jnp.mean(dxh * xh, axis=-1, keepdims=True)))
                xhs.append(xh)
            xh = jnp.concatenate(xhs, axis=1)
            do_ref[...] = jnp.concatenate(outs, axis=1)
            dp_ref[...] = (dy_ * xh * w * (sg * (1.0 + g * (1.0 - sg)))).astype(BF16)
            dw_ref[...] += jnp.sum(dsw * xh, axis=0, keepdims=True)

    tok = pl.BlockSpec((TM, HGW), lambda i: (i, 0))
    lat = pl.BlockSpec((TM, HGW), lambda i: (jnp.maximum(i - 1, 0), 0))
    return _pcall(body, name="readout_bwd", grid=(tt // TM,),
                  in_specs=[tok, tok, pl.BlockSpec((TM, HGW), lambda i: (i, C_GHG)), _full((1, HGW)), lat, ANY],
                  out_specs=[pl.BlockSpec((TM, HGW), lambda i: (i, C_GHG)), lat, _full((1, HGW))],
                  out_shape=[jax.ShapeDtypeStruct(dp.shape, BF16), jax.ShapeDtypeStruct((s_len, HGW), F32),
                             jax.ShapeDtypeStruct((1, HGW), F32)],
                  aliases={5: 0}, carry=carry)(o0, o1, p, hw4, dy, dp)


def _rope_tables(s_len):
    t = np.arange(s_len)
    inv = ROPE_THETA ** (-np.arange(0, 32, 2, dtype=np.float64) / 32)
    def half(pos):
        ang = pos[:, None].astype(np.float64) * inv[None, :]
        return (np.concatenate([np.cos(ang), np.cos(ang)], 1), np.concatenate([-np.sin(ang), np.sin(ang)], 1))
    cr, sr = half(t // GRID_W)
    cc, sc = half(t % GRID_W)
    cos = np.concatenate([cr, cc, cr, cc], 1)
    sin = np.concatenate([sr, sc, sr, sc], 1)
    cos = np.concatenate([np.ones((L, 128)), cos], 0)
    sin = np.concatenate([np.zeros((L, 128)), sin], 0)
    return jnp.asarray(cos, F32), jnp.asarray(sin, F32)


def _blockdiag(n, w):
    i = np.arange(n)
    return jnp.asarray((i[:, None] // w == i[None, :] // w) / float(w), F32)


def _dup_matrix():
    m = np.zeros((128, 512), np.float32)
    for g in range(2):
        for j in range(4):
            for dd in range(HDIM):
                m[64 * g + dd, 256 * g + 64 * j + dd] = 1.0
    return m


def _head_mean(x, blockdiag):
    return _dot(x, blockdiag, prec=lax.Precision.HIGH)


def _rot(x):
    n = x.shape[1]
    lane = lax.broadcasted_iota(jnp.int32, x.shape, 1)
    return jnp.where((lane % 32) < 16, pltpu.roll(x, n - 16, 1), pltpu.roll(x, 16, 1))


def _qk_prep(p, cos, sin, qnw8, knw2, bd512, bd128, dup):
    tt = p.shape[0]

    def body(q_ref, kv_ref, cos_ref, sin_ref, qw_ref, kw_ref, b5_ref, b1_ref, dup_ref,
             qr_ref, k4_ref, v4_ref):
        cos_, sin_ = cos_ref[...], sin_ref[...]
        q = q_ref[...]
        qn = q * lax.rsqrt(_head_mean(q * q, b5_ref[...]) + EPS) * qw_ref[...]
        cos4 = jnp.concatenate([cos_] * 4, axis=1)
        sin4 = jnp.concatenate([sin_] * 4, axis=1)
        qr_ref[...] = ((qn * cos4 + _rot(qn) * sin4) * (HDIM ** -0.5)).astype(BF16)
        kv = kv_ref[...]
        k, v = kv[:, :128], kv[:, 128:]
        kn = k * lax.rsqrt(_head_mean(k * k, b1_ref[...]) + EPS) * kw_ref[...]
        kr = kn * cos_ + _rot(kn) * sin_
        k4_ref[...] = _bdot(kr, dup_ref[...]).astype(BF16)
        v4_ref[...] = _bdot(v, dup_ref[...]).astype(BF16)

    row = lambda w, cb: pl.BlockSpec((TM, w), lambda i: (i, cb))
    out = jax.ShapeDtypeStruct((tt, ATW), BF16)
    return _pcall(body, name="qk_prep", grid=(tt // TM,),
                  in_specs=[row(ATW, C_QRAW), row(256, C_KV), row(128, 0), row(128, 0),
                            _full((1, ATW)), _full((1, 128)), _full((ATW, ATW)), _full((128, 128)),
                            _full((128, ATW))],
                  out_specs=[row(ATW, 0)] * 3, out_shape=[out] * 3)(
                      p, p, cos, sin, qnw8, knw2, bd512, bd128, dup)


def _attn_masks(i, nb):
    r = lax.broadcasted_iota(jnp.int32, (4 * BLK, 3 * BLK + L), 0) % BLK
    c = lax.broadcasted_iota(jnp.int32, (4 * BLK, 3 * BLK + L), 1)
    kpos = (i - 1) * BLK + c
    loc = (jnp.abs(c - BLK - r) <= BLK) & (kpos >= 0) & (kpos < nb * BLK)
    return loc | (c >= 3 * BLK)


def _stack_mask():
    r = lax.broadcasted_iota(jnp.int32, (4 * BLK, 256), 0)
    lane = lax.broadcasted_iota(jnp.int32, (4 * BLK, 256), 1)
    return (r // BLK) == (lane // HDIM)


def _stack_heads(xg, fill=0.0):
    x4 = jnp.concatenate([xg] * 4, axis=0)
    return jnp.where(_stack_mask(), x4, jnp.full_like(x4, fill))


def _unstack_heads(x4):
    out = jnp.where(_lane_mask(0), x4[0:BLK], 0.0)
    for j in range(1, 4):
        out = out + jnp.where(_lane_mask(j), x4[j * BLK:(j + 1) * BLK], 0.0)
    return out


def _per_head_rows(vals):
    return jnp.concatenate([jnp.broadcast_to(v, (BLK, 1)) for v in vals], axis=0)


def _lane_mask(j):
    lane = lax.broadcasted_iota(jnp.int32, (1, 256), 1)
    return (lane // HDIM) == j


def _attn_specs(nb):
    blk = lambda off: pl.BlockSpec((BLK, ATW), lambda i: (jnp.clip(i + off, 0, nb - 1) + 2, 0))
    ctx = pl.BlockSpec((L, ATW), lambda i: (0, 0))
    return blk, ctx


def _attn_fwd(qr, k4, v4, sinks, carry=None):
    tt = qr.shape[0]
    s_len = tt - L
    nb = s_len // BLK

    def body(sk_ref, q_ref, kp, ko, kn, kc, vp, vo, vn, vc, y_ref, lse_ref):
        i = pl.program_id(0)
        valid = _attn_masks(i, nb)
        q = q_ref[...]
        ys, lses = [], []
        for g in range(2):
            gs = slice(256 * g, 256 * g + 256)
            kcat = jnp.concatenate([kp[:, gs], ko[:, gs], kn[:, gs], kc[:, gs]], axis=0)
            vcat = jnp.concatenate([vp[:, gs], vo[:, gs], vn[:, gs], vc[:, gs]], axis=0)
            sink4 = _per_head_rows([sk_ref[4 * g + j] for j in range(4)])
            q4 = _stack_heads(q[:, gs])
            o_parts, l_parts = [], []
            for hp in range(2):
                rows = slice(2 * BLK * hp, 2 * BLK * (hp + 1))
                sink = sink4[rows]
                s = jnp.where(valid[rows], _dot(q4[rows], kcat, NT), -1e30)
                m = jnp.maximum(jnp.max(s, axis=-1, keepdims=True), sink)
                e = jnp.exp(s - m)
                den = jnp.sum(e, axis=-1, keepdims=True) + jnp.exp(sink - m)
                o_parts.append(_bdot(e * (1.0 / den), vcat))
                l_parts.append(jnp.broadcast_to(m + jnp.log(den), (2 * BLK, 256)))
            ys.append(_unstack_heads(jnp.concatenate(o_parts, axis=0)))
            lses.append(_unstack_heads(jnp.concatenate(l_parts, axis=0)))
        y_ref[...] = jnp.concatenate(ys, axis=1).astype(BF16)
        lse_ref[...] = jnp.concatenate(lses, axis=1)

    blk, ctx = _attn_specs(nb)
    out = pl.BlockSpec((BLK, ATW), lambda i: (i, 0))
    return _pcall(body, name="attn_fwd", grid=(nb,),
                  in_specs=[pl.BlockSpec(memory_space=pltpu.SMEM), blk(0),
                            blk(-1), blk(0), blk(1), ctx, blk(-1), blk(0), blk(1), ctx],
                  out_specs=[out, out],
                  out_shape=[jax.ShapeDtypeStruct((s_len, ATW), BF16),
                             jax.ShapeDtypeStruct((s_len, ATW), F32)], carry=carry)(
                      sinks, qr, k4, k4, k4, k4, v4, v4, v4, v4)


def _attn_bwd(qr, k4, v4, sinks, y, lse, dy, carry=None):
    tt = qr.shape[0]
    s_len = tt - L
    nb = s_len // BLK

    def body(sk_ref, q_ref, kp, ko, kn, kc, vp, vo, vn, vc, y_ref, lse_ref, dy_ref,
             dq_ref, dkw_ref, dvw_ref, dkc_ref, dvc_ref, dsk_ref):
        i = pl.program_id(0)

        @pl.when(i == 0)
        def _():
            dkc_ref[...] = jnp.zeros_like(dkc_ref)
            dvc_ref[...] = jnp.zeros_like(dvc_ref)
            dsk_ref[...] = jnp.zeros_like(dsk_ref)

        valid = _attn_masks(i, nb)
        q = q_ref[...]
        dy_ = dy_ref[...]
        dly = dy_ * y_ref[...].astype(F32)
        lse_ = lse_ref[...]
        dqs = []
        for g in range(2):
            gs = slice(256 * g, 256 * g + 256)
            kcat = jnp.concatenate([kp[:, gs], ko[:, gs], kn[:, gs], kc[:, gs]], axis=0)
            vcat = jnp.concatenate([vp[:, gs], vo[:, gs], vn[:, gs], vc[:, gs]], axis=0)
            q4 = _stack_heads(q[:, gs])
            dy4 = _stack_heads(dy_[:, gs]).astype(BF16)
            lse4 = jnp.max(_stack_heads(lse_[:, gs], fill=-1e30), axis=-1, keepdims=True)
            delta = jnp.sum(_stack_heads(dly[:, gs]), axis=-1, keepdims=True)
            sink = _per_head_rows([sk_ref[4 * g + j] for j in range(4)])
            pr = jnp.where(valid, jnp.exp(_dot(q4, kcat, NT) - lse4), 0.0)
            dsb = (pr * (_dot(dy4, vcat, NT) - delta)).astype(BF16)
            dsink = jnp.exp(sink - lse4) * delta
            for j in range(4):
                dsk_ref[4 * g + j:4 * g + j + 1, :] += jnp.broadcast_to(
                    -jnp.sum(dsink[j * BLK:(j + 1) * BLK], axis=0, keepdims=True), (1, 128))
            dqs.append(_unstack_heads(_dot(dsb, kcat)))
            dkg = _dot(dsb, q4, TN)
            dvg = _dot(pr.astype(BF16), dy4, TN)
            dkw_ref[0, :, gs] = dkg[:3 * BLK]
            dvw_ref[0, :, gs] = dvg[:3 * BLK]
            dkc_ref[:, gs] += dkg[3 * BLK:]
            dvc_ref[:, gs] += dvg[3 * BLK:]
        dq_ref[...] = jnp.concatenate(dqs, axis=1)

    blk, ctx = _attn_specs(nb)
    out = pl.BlockSpec((BLK, ATW), lambda i: (i, 0))
    win = pl.BlockSpec((1, 3 * BLK, ATW), lambda i: (i, 0, 0))
    acc = _full((L, ATW))
    return _pcall(body, name="attn_bwd", grid=(nb,),
                  in_specs=[pl.BlockSpec(memory_space=pltpu.SMEM), blk(0),
                            blk(-1), blk(0), blk(1), ctx, blk(-1), blk(0), blk(1), ctx, out, out, out],
                  out_specs=[out, win, win, acc, acc, _full((8, 128))],
                  out_shape=[jax.ShapeDtypeStruct((s_len, ATW), F32),
                             jax.ShapeDtypeStruct((nb, 3 * BLK, ATW), F32),
                             jax.ShapeDtypeStruct((nb, 3 * BLK, ATW), F32),
                             jax.ShapeDtypeStruct((L, ATW), F32), jax.ShapeDtypeStruct((L, ATW), F32),
                             jax.ShapeDtypeStruct((8, 128), F32)], carry=carry)(
                      sinks, qr, k4, k4, k4, k4, v4, v4, v4, v4, y, lse, dy)


def _attn_post(p, cos, sin, qnw8, knw2, bd512, bd128, dupt, dq, dkw, dvw, dkc, dvc, dp, carry=None):
    tt = p.shape[0]
    s_len = tt - L
    nb = s_len // BLK
    nctx = L // BLK

    def body(q_ref, kv_ref, cos_ref, sin_ref, qw_ref, kw_ref, b5_ref, b1_ref, dupt_ref,
             dq_ref, kwp, kwo, kwn, vwp, vwo, vwn, dkc_ref, dvc_ref, _dp_in,
             dp_ref, dqw_ref, dkw_ref):
        t = pl.program_id(0)
        j = t - nctx

        @pl.when(t == 0)
        def _():
            dqw_ref[...] = jnp.zeros_like(dqw_ref)
            dkw_ref[...] = jnp.zeros_like(dkw_ref)

        is_lat = t >= nctx
        cos_, sin_ = cos_ref[...], sin_ref[...]
        has_p = is_lat & (j >= 1)
        has_n = is_lat & (j <= nb - 2)
        dk4 = (jnp.where(is_lat, kwo[0], dkc_ref[...]) + jnp.where(has_p, kwp[0], 0.0)
               + jnp.where(has_n, kwn[0], 0.0))
        dv4 = (jnp.where(is_lat, vwo[0], dvc_ref[...]) + jnp.where(has_p, vwp[0], 0.0)
               + jnp.where(has_n, vwn[0], 0.0))
        dkr = _dot(dk4, dupt_ref[...], prec=HI)
        dv = _dot(dv4, dupt_ref[...], prec=HI)
        kv = kv_ref[...]
        k = kv[:, :128]
        kw = kw_ref[...]
        rk = lax.rsqrt(_head_mean(k * k, b1_ref[...]) + EPS)
        xk = k * rk
        dkn = dkr * cos_ + _rot(dkr * sin_)
        dxk = dkn * kw
        dk = rk * (dxk - xk * _head_mean(dxk * xk, b1_ref[...]))
        dkw_ref[...] += jnp.sum(dkn * xk, axis=0, keepdims=True)
        q = q_ref[...]
        qw = qw_ref[...]
        rq = lax.rsqrt(_head_mean(q * q, b5_ref[...]) + EPS)
        xq = q * rq
        cos4 = jnp.concatenate([cos_] * 4, axis=1)
        sin4 = jnp.concatenate([sin_] * 4, axis=1)
        dqr = jnp.where(is_lat, dq_ref[...], 0.0) * (HDIM ** -0.5)
        dqn = dqr * cos4 + _rot(dqr * sin4)
        dxq = dqn * qw
        dqraw = rq * (dxq - xq * _head_mean(dxq * xq, b5_ref[...]))
        dqw_ref[...] += jnp.sum(dqn * xq, axis=0, keepdims=True)
        dp_ref[...] = jnp.concatenate([dqraw, dk, dv], axis=1).astype(BF16)

    row = lambda w, cb: pl.BlockSpec((BLK, w), lambda t: (t, cb))
    lat = pl.BlockSpec((BLK, ATW), lambda t: (jnp.maximum(t - nctx, 0), 0))

    def part(off):
        return pl.BlockSpec((1, BLK, ATW), lambda t: (jnp.clip(t - nctx + off, 0, nb - 1), 1 - off, 0))

    cacc = pl.BlockSpec((BLK, ATW), lambda t: (jnp.minimum(t, nctx - 1), 0))
    return _pcall(body, name="attn_post", grid=(tt // BLK,),
                  in_specs=[row(ATW, C_QRAW), row(256, C_KV), row(128, 0), row(128, 0),
                            _full((1, ATW)), _full((1, 128)), _full((ATW, ATW)), _full((128, 128)),
                            _full((ATW, 128)), lat, part(-1), part(0), part(1), part(-1), part(0), part(1),
                            cacc, cacc, ANY],
                  out_specs=[pl.BlockSpec((BLK, 768), lambda t: (t, C_QKV)), _full((1, ATW)), _full((1, 128))],
                  out_shape=[jax.ShapeDtypeStruct(dp.shape, BF16), jax.ShapeDtypeStruct((1, ATW), F32),
                             jax.ShapeDtypeStruct((1, 128), F32)],
                  aliases={18: 0}, carry=carry)(p, p, cos, sin, qnw8, knw2, bd512, bd128, dupt,
                                   dq, dkw, dkw, dkw, dvw, dvw, dvw, dkc, dvc, dp)


def _branch_merge(y_hg, y_at, bh4, ba4, p):
    s_len = y_hg.shape[0]

    def body(yh_ref, ya_ref, bh_ref, ba_ref, gh_ref, ga_ref, ah_ref, aa_ref, m_ref):
        yh, ya = yh_ref[...], ya_ref[...]
        ah = jnp.concatenate([_bdot(yh, bh_ref[j]) for j in range(4)], axis=1)
        aa = jnp.concatenate([_bdot(ya, ba_ref[j]) for j in range(4)], axis=1)
        ah_ref[...] = ah
        aa_ref[...] = aa
        m_ref[...] = (_sig(gh_ref[...]) * ah + _sig(ga_ref[...]) * aa).astype(BF16)

    row = pl.BlockSpec((TM, D), lambda i: (i, 0))
    y = pl.BlockSpec((TM, HGW), lambda i: (i, 0))
    f = jax.ShapeDtypeStruct((s_len, D), F32)
    return _pcall(body, name="branch_merge", grid=(s_len // TM,),
                  in_specs=[y, y, _full(bh4.shape), _full(ba4.shape),
                            pl.BlockSpec((TM, D), lambda i: (i + 1, 2)), pl.BlockSpec((TM, D), lambda i: (i + 1, 3))],
                  out_specs=[row, row, row],
                  out_shape=[f, f, jax.ShapeDtypeStruct((s_len, D), BF16)])(y_hg, y_at, bh4, ba4, p, p)


def _branch_bwd(dmh, dma, bh4, ba4, y_hg, y_at):
    s_len = dmh.shape[0]
    nk = s_len // TS
    ns = D // 4

    def body(dh_ref, da_ref, bh_ref, ba_ref, yh_ref, ya_ref, dyh_ref, dya_ref, gh_ref, ga_ref, acc_h, acc_a):
        t = pl.program_id(0)

        @pl.when(t == 0)
        def _():
            acc_h[...] = jnp.zeros_like(acc_h)
            acc_a[...] = jnp.zeros_like(acc_a)

        for d_ref, w_ref, y_ref, dy_ref, acc in ((dh_ref, bh_ref, yh_ref, dyh_ref, acc_h),
                                                 (da_ref, ba_ref, ya_ref, dya_ref, acc_a)):
            y = y_ref[...]
            dy = jnp.zeros((TS, HGW), F32)
            for j in range(4):
                dj = d_ref[:, j * ns:(j + 1) * ns]
                dy = dy + _bdot(dj, w_ref[j], NT)
                acc[j] += _bdot(y, dj, TN)
            dy_ref[...] = dy

        @pl.when(t == nk - 1)
        def _():
            gh_ref[...] = acc_h[...].astype(BF16)
            ga_ref[...] = acc_a[...].astype(BF16)

    dm = pl.BlockSpec((TS, D), lambda t: (t, 0))
    y = pl.BlockSpec((TS, HGW), lambda t: (t, 0))
    w = _full(bh4.shape)
    fy = jax.ShapeDtypeStruct((s_len, HGW), F32)
    gw = jax.ShapeDtypeStruct(bh4.shape, BF16)
    return _pcall(body, name="branch_bwd", grid=(nk,), in_specs=[dm, dm, w, w, y, y],
                  out_specs=[y, y, w, w], out_shape=[fy, fy, gw, gw],
                  scratch=[pltpu.VMEM(bh4.shape, F32)] * 2)(dmh, dma, bh4, ba4, y_hg, y_at)


def _merge_bwd(dattn, w_o, mixed, ah, aa, p, carry=None):
    tt = p.shape[0]
    s_len = tt - L
    nt = tt // TM

    def body(da_ref, wo_ref, mx_ref, ah_ref, aa_ref, gh_ref, ga_ref, dp_ref, dmh_ref, dma_ref, go_ref, acc):
        i = pl.program_id(0)

        @pl.when(i == 0)
        def _():
            dp_ref[...] = jnp.zeros_like(dp_ref)
            acc[...] = jnp.zeros_like(acc)

        @pl.when(i >= 1)
        def _():
            da = da_ref[...]
            acc[...] += _bdot(mx_ref[...], da, TN)
            dm_ = _bdot(da, wo_ref[...], NT)
            sh, sa = _sig(gh_ref[...]), _sig(ga_ref[...])
            dp_ref[...] = jnp.concatenate([dm_ * ah_ref[...] * sh * (1.0 - sh),
                                           dm_ * aa_ref[...] * sa * (1.0 - sa)], axis=1).astype(BF16)
            dmh_ref[...] = (dm_ * sh).astype(BF16)
            dma_ref[...] = (dm_ * sa).astype(BF16)

        @pl.when(i == nt - 1)
        def _():
            go_ref[...] = acc[...].astype(BF16)

    lat = pl.BlockSpec((TM, D), lambda i: (jnp.maximum(i - 1, 0), 0))
    return _pcall(body, name="merge_bwd", grid=(nt,),
                  in_specs=[lat, _full((D, D)), lat, lat, lat, pl.BlockSpec((TM, D), lambda i: (i, 2)),
                            pl.BlockSpec((TM, D), lambda i: (i, 3))],
                  out_specs=[pl.BlockSpec((TM, 2 * D), lambda i: (i, C_GATES)), lat, lat, _full((D, D))],
                  out_shape=[jax.ShapeDtypeStruct((tt, NCOL), BF16), jax.ShapeDtypeStruct((s_len, D), BF16),
                             jax.ShapeDtypeStruct((s_len, D), BF16), jax.ShapeDtypeStruct((D, D), BF16)],
                  scratch=[pltpu.VMEM((D, D), F32)], carry=carry)(dattn, w_o, mixed, ah, aa, p, p)


def _local_step(x, ctx, tgt, mod, modc, nw1, nw2, lg, hw, qnw, knw, sinks,
                w_in, wts, dist=None):
    s_len = x.shape[0]
    tt = s_len + L
    ss1 = jnp.stack([modc, mod[0:2]])
    ss2 = mod[3:5][None]
    g1, g2 = mod[2:3], mod[5:6]
    hw4 = jnp.tile(hw, (1, 4))
    qnw8 = jnp.tile(qnw, (1, 8))
    knw2 = jnp.tile(knw, (1, 2))
    cos, sin = _rope_tables(s_len)
    bd512, bd128 = _blockdiag(ATW, HDIM), _blockdiag(128, HDIM)
    dupm = _dup_matrix()
    dup, dupt = jnp.asarray(dupm, BF16), jnp.asarray(dupm.T, F32)
    tmt = tt

    def four(b):
        return b.reshape(4, 2 * b.shape[1], b.shape[2])

    def halves(g):
        return g.reshape(4, 2, g.shape[1] // 2, g.shape[2])

    h = _mod1(ctx, x, nw1, ss1)
    if dist is None:
        bh4, ba4, w_o, g4, u4, dn4 = wts
        p = _mm_in(h, w_in, tmt)
        o0, st0 = _hgrn_fwd(p, lg, rev=False)
        o1, st1, y_hg = _hgrn_fwd(p, lg, rev=True, readout=(o0, hw4))
    else:
        core, chip = dist
        half = wts[3].shape[1] // 2
        p, first = _mm_in(h, w_in, tmt, carry=_carry_join(_carry_gather(list(wts[0:3])),
                                                          _carry_gather([wts[3]], rows=[(0, half)])))
        (o0, st0), (g8,) = _hgrn_fwd(p, lg, rev=False, carry=_carry_gather([first[3]], rows=[(half, half)]))
        (o1, st1, y_hg), (dn8a,) = _hgrn_fwd(p, lg, rev=True, readout=(o0, hw4),
                                             carry=_carry_gather([wts[5]], rows=[(0, half)]))
        bh4, ba4, w_o, g4 = four(first[0]), four(first[1]), four(first[2]).reshape(D, D), four(g8)
    qr, k4, v4 = _qk_prep(p, cos, sin, qnw8, knw2, bd512, bd128, dup)
    if dist is None:
        y_at, lse = _attn_fwd(qr, k4, v4, sinks)
    else:
        (y_at, lse), (u8, dn8) = _attn_fwd(qr, k4, v4, sinks,
                                           carry=_carry_gather([wts[4], dn8a], rows=[None, (half, half)]))
        u4, dn4 = four(u8), four(dn8)
    ah, aa, mixed = _branch_merge(y_hg, y_at, bh4, ba4, p)
    ao, x1, h2 = _out_proj_mod2(mixed, w_o, x, g1, nw2, ss2)
    a4, b4, z4 = _ffn_up(h2, g4, u4)
    sq, dx2, dyb, dg2 = _ffn_down_loss(z4, dn4, x1, g2, tgt)

    da4, db4 = _ffn_dz(dyb, dn4, a4, b4)
    g_dn = _ffn_gdn(z4, dyb)
    if dist is None:
        dh2 = _ffn_dh2(da4, db4, g4, u4)
    else:
        dn_units = [halves(g_dn)]
        dh2, dn_recv = _ffn_dh2(da4, db4, g4, u4, carry=_carry_pairx(dn_units))
        dn_pairs = _rs_pair_add(dn_units, dn_recv, core)
    if dist is None:
        g_g, g_u = _ffn_ggu(h2, da4, db4)
    else:
        (g_g, g_u), c_dn = _ffn_ggu(h2, da4, db4, carry=_carry_chipx(dn_pairs))
        red_dn = _rs_chip_add(dn_pairs, c_dn, core, chip)
    dx1, dattn, dss2, dnw2, dg1 = _mod2_bwd(x1, dh2, dx2, ao, nw2, ss2, g1)
    if dist is None:
        dp, dmh, dma, g_o = _merge_bwd(dattn, w_o, mixed, ah, aa, p)
    else:
        gu_units = [halves(g_g), halves(g_u)]
        (dp, dmh, dma, g_o), gu_recv = _merge_bwd(dattn, w_o, mixed, ah, aa, p, carry=_carry_pairx(gu_units))
        ffn_pairs = list(dn_pairs) + list(_rs_pair_add(gu_units, gu_recv, core))
    dy_hg, dy_at, g_bh, g_ba = _branch_bwd(dmh, dma, bh4, ba4, y_hg, y_at)
    if dist is None:
        dp, do, dhw4 = _readout_bwd(o0, o1, p, hw4, dy_hg, dp)
        dq, dkw, dvw, dkc, dvc, dsk = _attn_bwd(qr, k4, v4, sinks, y_at, lse, dy_at)
        dp, dqnw8, dknw2 = _attn_post(p, cos, sin, qnw8, knw2, bd512, bd128, dupt, dq, dkw, dvw, dkc, dvc, dp)
    else:
        mix_units = [halves(g_bh), halves(g_ba), halves(g_o.reshape(4, D // 4, D))]
        (dp, do, dhw4), mix_recv = _readout_bwd(o0, o1, p, hw4, dy_hg, dp, carry=_carry_pairx(mix_units))
        mix_pairs = _rs_pair_add(mix_units, mix_recv, core)
        (dq, dkw, dvw, dkc, dvc, dsk), bwd = _attn_bwd(
            qr, k4, v4, sinks, y_at, lse, dy_at,
            carry=_carry_join(_carry_chipx(ffn_pairs[1:2]), _carry_sibx(red_dn)))
        red_g = _rs_chip_add(ffn_pairs[1:2], bwd[0:1], core, chip)
        (dp, dqnw8, dknw2), post = _attn_post(
            p, cos, sin, qnw8, knw2, bd512, bd128, dupt, dq, dkw, dvw, dkc, dvc, dp, carry=_carry_sibx(red_g))
    if dist is None:
        dp, dv0, dq0, dlg0 = _hgrn_bwd(p, lg, do, st0, dp, None, rev=False)
        dp, dlg1 = _hgrn_bwd(p, lg, do, st1, dp, (dv0, dq0), rev=True)
    else:
        (dp, dv0, dq0, dlg0), c_u = _hgrn_bwd(p, lg, do, st0, dp, None, rev=False,
                                              carry=_carry_chipx(ffn_pairs[2:3]))
        red_u = _rs_chip_add(ffn_pairs[2:3], c_u, core, chip)
        (dp, dlg1), last = _hgrn_bwd(p, lg, do, st1, dp, (dv0, dq0), rev=True,
                                     carry=_carry_join(_carry_chipx(mix_pairs), _carry_sibx(red_u)))
        mix_reds = _rs_chip_add(mix_pairs, last[0:3], core, chip)
        ffn_done = bwd[1:2] + post[0:1] + last[3:4]
    g_in = _mm_gin(dp, h, tmt)
    if dist is None:
        dh = _mm_dh(dp, w_in, tmt)
        gx, dss1, dnw1 = _mod1_bwd(ctx, x, dh, dx1, nw1, ss1)
        rs = None
    else:
        in_units = [halves(g_in.reshape(4, NCOL // 4, D))]
        dh, both = _mm_dh(dp, w_in, tmt, carry=_carry_join(_carry_pairx(in_units), _carry_sibx(mix_reds)))
        in_recv, mix_done = both[0:1], both[1:4]
        in_pairs = _rs_pair_add(in_units, in_recv, core)
        first_rows = (0, in_pairs[0].shape[1] // 2)
        (gx, dss1, dnw1), in_part = _mod1_bwd(ctx, x, dh, dx1, nw1, ss1,
                                              carry=_carry_chipx(in_pairs, rows=first_rows))
        rs = dict(ffn_done=ffn_done, mix_done=mix_done, in_pairs=in_pairs, in_part=in_part)

    dmod = jnp.concatenate([dss1[1], dg1, dss2, dg2], axis=0)
    dmodc = dss1[0]
    raw = (dss1, dg1, dss2, dg2, dnw1, dnw2, dhw4, dqnw8, dknw2, dsk, dlg0, dlg1)
    small = dict(raw=raw, dmod=dmod, dmodc=dmodc, dnw1=dnw1, dnw2=dnw2,
                 dhw=dhw4.reshape(4, HGD).sum(0, keepdims=True),
                 dqnw=dqnw8.reshape(8, HDIM).sum(0, keepdims=True),
                 dknw=dknw2.reshape(2, HDIM).sum(0, keepdims=True),
                 dsinks=dsk[:, 0], dlg=jnp.concatenate([dlg0, dlg1], axis=0))
    big = dict(w_in=g_in, w_bh=g_bh, w_ba=g_ba, w_o=g_o, w_g=g_g, w_u=g_u, w_dn=g_dn)
    return sq, gx, big, small, rs


def _place():
    x, y, c = lax.axis_index("x"), lax.axis_index("y"), lax.axis_index("c")
    return x, y, c


def _gather_blocks(x_refs, out_refs, send_sems, recv_sems, local_sems):
    n = len(out_refs)
    x, y, c = _place()
    me, sibling = (x, y, c), (x, y, 1 - c)
    chips = [(1 - x, y), (x, 1 - y), (1 - x, 1 - y)]

    def slot(u, px, py, pc):
        return out_refs[u].at[4 * px + 2 * py + pc]

    def copy(u, k, block, to, src=None):
        return pltpu.make_async_remote_copy(
            src_ref=slot(u, *block) if src is None else src, dst_ref=slot(u, *block),
            send_sem=send_sems.at[u, k], recv_sem=recv_sems.at[u, k], device_id=to, device_id_type=MESH)

    mines = []
    if x_refs is not None:
        mines = [pltpu.make_async_copy(x_refs[u], slot(u, *me), local_sems.at[u]) for u in range(n)]
    for cp in mines:
        cp.start()
    first = []
    for u in range(n):
        src = None if x_refs is None else x_refs[u]
        first.append(copy(u, 0, me, sibling, src=src))
        first += [copy(u, 1 + j, me, (*chip, c), src=src) for j, chip in enumerate(chips)]
    for cp in first:
        cp.start()
    passed = []
    for j, chip in enumerate(chips):
        for u in range(n):
            copy(u, 1 + j, (*chip, c), me).wait_recv()
            fwd = copy(u, 4 + j, (*chip, c), sibling)
            fwd.start()
            passed.append(fwd)
    for u in range(n):
        copy(u, 0, sibling, me).wait_recv()
    for j, chip in enumerate(chips):
        for u in range(n):
            copy(u, 4 + j, (*chip, 1 - c), me).wait_recv()
    for cp in first + passed:
        cp.wait_send()
    for cp in mines:
        cp.wait()


def _gather_sems(n):
    return [pltpu.SemaphoreType.DMA((n, 7)), pltpu.SemaphoreType.DMA((n, 7)), pltpu.SemaphoreType.DMA((n,))]


def _allgather(blks, *, name, in_vmem):
    n = len(blks)
    space = pltpu.VMEM if in_vmem else pl.ANY

    def body(*refs):
        _gather_blocks(refs[:n], refs[n:2 * n], *refs[2 * n:])

    return pl.pallas_call(
        body, name=name, out_shape=[jax.ShapeDtypeStruct((8,) + b.shape, b.dtype) for b in blks],
        in_specs=[pl.BlockSpec(memory_space=space)] * n, out_specs=[pl.BlockSpec(memory_space=space)] * n,
        scratch_shapes=_gather_sems(n))(*blks)


def _cast_place(ws, c, dev):
    n = len(ws)

    def body(s_ref, *refs):
        for u in range(n):
            refs[n + u][0] = refs[u][...].astype(BF16)

    in_specs, out_specs, out_shape = [], [], []
    for w in ws:
        q, cols = w.shape[0] // 4, w.shape[1]
        in_specs.append(pl.BlockSpec((q, cols), lambda i, s: (2 * s[0] + i, 0)))
        out_specs.append(pl.BlockSpec((1, q, cols), lambda i, s: (s[1], i, 0)))
        out_shape.append(jax.ShapeDtypeStruct((8, 2 * q, cols), BF16))
    return pl.pallas_call(
        body, name="cast_place",
        grid_spec=pltpu.PrefetchScalarGridSpec(num_scalar_prefetch=1, grid=(2,), in_specs=in_specs,
                                               out_specs=out_specs),
        out_shape=_out_hbm(out_shape),
        compiler_params=pltpu.CompilerParams(vmem_limit_bytes=48 << 20))(jnp.stack([c, dev]), *_in_hbm(ws))


def _gather_phases(out_refs, send_sems, recv_sems, rows=None):
    n = len(out_refs)
    x, y, c = _place()
    me, sibling = (x, y, c), (x, y, 1 - c)
    chips = [(1 - x, y), (x, 1 - y), (1 - x, 1 - y)]

    def copy(u, k, block, to):
        px, py, pc = block
        ref = out_refs[u].at[4 * px + 2 * py + pc]
        if rows is not None and rows[u] is not None:
            ref = ref.at[pl.ds(rows[u][0], rows[u][1])]
        return pltpu.make_async_remote_copy(src_ref=ref, dst_ref=ref, send_sem=send_sems.at[u, k],
                                            recv_sem=recv_sems.at[u, k], device_id=to, device_id_type=MESH)

    def start():
        for u in range(n):
            copy(u, 0, me, sibling).start()
            for j, chip in enumerate(chips):
                copy(u, 1 + j, me, (*chip, c)).start()

    def mid():
        for j, chip in enumerate(chips):
            for u in range(n):
                copy(u, 1 + j, (*chip, c), me).wait_recv()
                copy(u, 4 + j, (*chip, c), sibling).start()

    def end():
        for u in range(n):
            copy(u, 0, sibling, me).wait_recv()
        for j, chip in enumerate(chips):
            for u in range(n):
                copy(u, 4 + j, (*chip, 1 - c), me).wait_recv()
        for u in range(n):
            copy(u, 0, me, sibling).wait_send()
            for j, chip in enumerate(chips):
                copy(u, 1 + j, me, (*chip, c)).wait_send()
                copy(u, 4 + j, (*chip, c), sibling).wait_send()

    return start, mid, end


def _carry_gather(bufs, rows=None):
    n = len(bufs)
    return _Carry(bufs, [jax.ShapeDtypeStruct(b.shape, b.dtype) for b in bufs], {u: u for u in range(n)},
                  [pltpu.SemaphoreType.DMA((n, 7)), pltpu.SemaphoreType.DMA((n, 7))],
                  lambda ins, outs, sems: _gather_phases(outs, *sems, rows=rows))


def _allgather_inplace(bufs, *, name):
    n = len(bufs)

    def body(*refs):
        for phase in _gather_phases(refs[n:2 * n], *refs[2 * n:]):
            phase()

    return pl.pallas_call(
        body, name=name, out_shape=[jax.ShapeDtypeStruct(b.shape, b.dtype) for b in bufs],
        in_specs=[ANY] * n, out_specs=[ANY] * n, input_output_aliases={u: u for u in range(n)},
        scratch_shapes=[pltpu.SemaphoreType.DMA((n, 7)), pltpu.SemaphoreType.DMA((n, 7))])(*bufs)


def _ag_small(raw, sq):
    def body(dss1, dg1, dss2, dg2, dnw1, dnw2, dhw4, dqnw8, dknw2, dsk, dlg0, dlg1, sq_ref,
             out_ref, tot_ref, blk, send_sems, recv_sems, local_sems):
        blk[...] = jnp.zeros_like(blk)
        blk[0:2, :] = dss1[1]
        blk[2:3, :] = dg1[...]
        blk[3:5, :] = dss2[...]
        blk[5:6, :] = dg2[...]
        blk[6:8, :] = dss1[0]
        blk[8:9, :] = dnw1[...]
        blk[9:10, :] = dnw2[...]
        blk[10:11, 0:HGW] = dhw4[...]
        blk[10:11, HGW:D] = dqnw8[...]
        blk[11:12, 0:128] = dknw2[...]
        blk[12:14, 0:HGW] = dlg0[0]
        blk[14:16, 0:HGW] = dlg1[0]
        blk[16:24, 0:128] = dsk[...]
        blk[24:25, :] = sq_ref[...]
        _gather_blocks([blk], [out_ref], send_sems, recv_sems, local_sems)
        acc = out_ref[0]
        for i in range(1, 8):
            acc = acc + out_ref[i]
        tot_ref[...] = acc

    vm = pl.BlockSpec(memory_space=pltpu.VMEM)
    return pl.pallas_call(
        body, name="ag_small",
        out_shape=[jax.ShapeDtypeStruct((8, 32, D), F32), jax.ShapeDtypeStruct((32, D), F32)],
        in_specs=[vm] * 13, out_specs=[vm, vm],
        scratch_shapes=[pltpu.VMEM((32, D), F32)] + _gather_sems(1))(*raw, sq)


def _rs_pair_exchange(units):
    n = len(units)

    def body(*refs):
        start, _, end = _pairx_phases(refs[:n], refs[n:2 * n], *refs[2 * n:])
        start()
        end()

    return pl.pallas_call(
        body, name="rs_pair_exchange", out_shape=_pairx_shapes(units),
        in_specs=[ANY] * n, out_specs=[ANY] * n,
        scratch_shapes=[pltpu.SemaphoreType.DMA((n, 4)), pltpu.SemaphoreType.DMA((n, 4))])(*units)


def _pairx_shapes(units):
    return [jax.ShapeDtypeStruct((4,) + g.shape[2:], g.dtype) for g in units]


def _pairx_phases(g_refs, r_refs, send_sems, recv_sems):
    n = len(g_refs)
    x, y, c = _place()
    cps = [pltpu.make_async_remote_copy(
        src_ref=g_refs[u].at[j, 1 - c], dst_ref=r_refs[u].at[j], send_sem=send_sems.at[u, j],
        recv_sem=recv_sems.at[u, j], device_id=(x, y, 1 - c), device_id_type=MESH)
        for u in range(n) for j in range(4)]

    def start():
        for cp in cps:
            cp.start()

    def end():
        for cp in cps:
            cp.wait()

    return start, None, end


def _carry_pairx(units):
    n = len(units)
    return _Carry(units, _pairx_shapes(units), {},
                  [pltpu.SemaphoreType.DMA((n, 4)), pltpu.SemaphoreType.DMA((n, 4))],
                  lambda ins, outs, sems: _pairx_phases(ins, outs, *sems))


def _rs_pair_add(units, recvs, c):
    n = len(units)

    def body(c_ref, *refs):
        for u in range(n):
            refs[2 * n + u][...] = (refs[u][0].astype(F32) + refs[n + u][...].astype(F32)).astype(BF16)

    in_specs, out_specs, out_shape = [], [], []
    for g in units:
        h, w = g.shape[2] // 2, g.shape[3]
        in_specs.append(pl.BlockSpec((1, 1, h, w), lambda j, i, cr: (j, cr[0], i, 0)))
    for g in units:
        h, w = g.shape[2] // 2, g.shape[3]
        in_specs.append(pl.BlockSpec((1, h, w), lambda j, i, cr: (j, i, 0)))
        out_specs.append(pl.BlockSpec((1, h, w), lambda j, i, cr: (j, i, 0)))
        out_shape.append(jax.ShapeDtypeStruct((4, 2 * h, w), BF16))
    return pl.pallas_call(
        body, name="rs_pair_add",
        grid_spec=pltpu.PrefetchScalarGridSpec(num_scalar_prefetch=1, grid=(4, 2), in_specs=in_specs,
                                               out_specs=out_specs),
        out_shape=_out_hbm(out_shape),
        compiler_params=pltpu.CompilerParams(vmem_limit_bytes=48 << 20))(
            c.reshape(1), *_in_hbm(list(units) + list(recvs)))


def _rs_chip_exchange(pairs):
    n = len(pairs)

    def body(*refs):
        start, _, end = _chipx_phases(refs[:n], refs[n:2 * n], *refs[2 * n:])
        start()
        end()

    return pl.pallas_call(
        body, name="rs_chip_exchange", out_shape=[jax.ShapeDtypeStruct(p.shape, p.dtype) for p in pairs],
        in_specs=[ANY] * n, out_specs=[ANY] * n,
        scratch_shapes=[pltpu.SemaphoreType.DMA((n, 3)), pltpu.SemaphoreType.DMA((n, 3))])(*pairs)


def _chipx_phases(p_refs, r_refs, send_sems, recv_sems, rows=None):
    n = len(p_refs)
    x, y, c = _place()
    k = 2 * x + y

    def part(ref):
        return ref if rows is None else ref.at[pl.ds(rows[0], rows[1])]

    sends = []
    for d in range(1, 4):
        j = (k + d) % 4
        for u in range(n):
            sends.append(pltpu.make_async_remote_copy(
                src_ref=part(p_refs[u].at[j]), dst_ref=part(r_refs[u].at[k]), send_sem=send_sems.at[u, d - 1],
                recv_sem=recv_sems.at[u, d - 1], device_id=(j // 2, j % 2, c), device_id_type=MESH))

    def start():
        for cp in sends:
            cp.start()

    def end():
        for d in range(1, 4):
            src = (k + 4 - d) % 4
            for u in range(n):
                pltpu.make_async_remote_copy(
                    src_ref=part(p_refs[u].at[src]), dst_ref=part(r_refs[u].at[src]),
                    send_sem=send_sems.at[u, d - 1], recv_sem=recv_sems.at[u, d - 1], device_id=(x, y, c),
                    device_id_type=MESH).wait_recv()
        for cp in sends:
            cp.wait_send()

    return start, None, end


def _carry_chipx(pairs, rows=None, into=None):
    n = len(pairs)
    sems = [pltpu.SemaphoreType.DMA((n, 3)), pltpu.SemaphoreType.DMA((n, 3))]
    shapes = [jax.ShapeDtypeStruct(p.shape, p.dtype) for p in pairs]
    if into is None:
        return _Carry(pairs, shapes, {}, sems, lambda ins, outs, s: _chipx_phases(ins, outs, *s, rows=rows))
    return _Carry(list(pairs) + list(into), shapes, {n + u: u for u in range(n)}, sems,
                  lambda ins, outs, s: _chipx_phases(ins[:n], outs, *s, rows=rows))


def _rs_chip_add(pairs, contribs, c, chip):
    n = len(pairs)

    def body(s_ref, *refs):
        for u in range(n):
            a, b, c_, d = refs[4 * u:4 * u + 4]
            refs[4 * n + u][0] = ((a[0].astype(F32) + b[0].astype(F32)) + c_[0].astype(F32)) + d[0].astype(F32)

    in_specs, out_specs, out_shape, args = [], [], [], []
    for p, r in zip(pairs, contribs):
        h, w = p.shape[1] // 2, p.shape[2]
        in_specs += [pl.BlockSpec((1, h, w), functools.partial(lambda d, i, s: ((s[1] + d) % 4, i, 0), d))
                     for d in range(4)]
        args += [p, r, r, r]
        out_specs.append(pl.BlockSpec((1, h, w), lambda i, s: (s[0], i, 0)))
        out_shape.append(jax.ShapeDtypeStruct((2, 2 * h, w), F32))
    return pl.pallas_call(
        body, name="rs_chip_add",
        grid_spec=pltpu.PrefetchScalarGridSpec(num_scalar_prefetch=1, grid=(2,), in_specs=in_specs,
                                               out_specs=out_specs),
        out_shape=_out_hbm(out_shape),
        compiler_params=pltpu.CompilerParams(vmem_limit_bytes=48 << 20))(jnp.stack([c, chip]), *_in_hbm(args))


def _rs_sibling_gather(reds):
    n = len(reds)

    def body(*refs):
        start, _, end = _sibx_phases(refs[n:2 * n], *refs[2 * n:])
        start()
        end()

    return pl.pallas_call(
        body, name="rs_sibling_gather", out_shape=[jax.ShapeDtypeStruct(r.shape, r.dtype) for r in reds],
        in_specs=[ANY] * n, out_specs=[ANY] * n, input_output_aliases={u: u for u in range(n)},
        scratch_shapes=[pltpu.SemaphoreType.DMA((n,))] * 2)(*reds)


def _sibx_phases(o_refs, send_sems, recv_sems):
    n = len(o_refs)
    x, y, c = _place()
    cps = [pltpu.make_async_remote_copy(
        src_ref=o_refs[u].at[c], dst_ref=o_refs[u].at[c], send_sem=send_sems.at[u], recv_sem=recv_sems.at[u],
        device_id=(x, y, 1 - c), device_id_type=MESH) for u in range(n)]

    def start():
        for cp in cps:
            cp.start()

    def end():
        for u in range(n):
            cps[u].wait_send()
            pltpu.make_async_remote_copy(
                src_ref=o_refs[u].at[1 - c], dst_ref=o_refs[u].at[1 - c], send_sem=send_sems.at[u],
                recv_sem=recv_sems.at[u], device_id=(x, y, 1 - c), device_id_type=MESH).wait_recv()

    return start, None, end


def _carry_sibx(reds):
    n = len(reds)
    return _Carry(reds, [jax.ShapeDtypeStruct(r.shape, r.dtype) for r in reds], {u: u for u in range(n)},
                  [pltpu.SemaphoreType.DMA((n,))] * 2, lambda ins, outs, sems: _sibx_phases(outs, *sems))


def _prologue(blk, c_ctx, w, b, in8):
    n = w.shape[1]

    def body(blk_ref, cctx_ref, w_ref, b_ref, _in_in, g0_ref, c16_ref, g1_ref, in_ref, mod_s,
             s1, r1, l1, s2, r2, l2, s3, r3):
        start, mid, end = _gather_phases([in_ref], s3, r3)
        start()
        _gather_blocks([blk_ref], [g0_ref], s1, r1, l1)
        c16 = jnp.concatenate([g0_ref[i, 0:1, :] for i in range(8)] + [cctx_ref[...], jnp.zeros((7, D), F32)],
                              axis=0)
        c16_ref[...] = c16
        mod_s[...] = _dot(c16 * _sig(c16), w_ref[...], prec=HI) + b_ref[...]
        _gather_blocks([mod_s], [g1_ref], s2, r2, l2)
        mid()
        end()

    vm = pl.BlockSpec(memory_space=pltpu.VMEM)
    return pl.pallas_call(
        body, name="prologue",
        out_shape=[jax.ShapeDtypeStruct((8, 8, D), F32), jax.ShapeDtypeStruct((16, D), F32),
                   jax.ShapeDtypeStruct((8, 16, n), F32), jax.ShapeDtypeStruct(in8.shape, in8.dtype)],
        in_specs=[vm, vm, vm, vm, ANY], out_specs=[vm, vm, vm, ANY], input_output_aliases={4: 3},
        scratch_shapes=[pltpu.VMEM((16, n), F32)] + _gather_sems(1) + _gather_sems(1)
        + [pltpu.SemaphoreType.DMA((1, 7)), pltpu.SemaphoreType.DMA((1, 7))],
        compiler_params=pltpu.CompilerParams(vmem_limit_bytes=48 << 20))(blk, c_ctx, w, b, in8)


def _ada_bwd(c16, dmod16, w):
    n = w.shape[1]
    tn = 512

    def body(c_ref, d_ref, w_ref, gw_ref, gc_ref):
        j = pl.program_id(0)

        @pl.when(j == 0)
        def _():
            gc_ref[...] = jnp.zeros_like(gc_ref)

        cc = c_ref[...]
        dm = d_ref[...]
        gw_ref[...] = _dot(cc * _sig(cc), dm, TN, prec=HI)
        gc_ref[...] += _dot(dm, w_ref[...], NT, prec=HI)

    return _pcall(body, name="ada_bwd", grid=(n // tn,),
                  in_specs=[_full((16, D)), pl.BlockSpec((16, tn), lambda j: (0, j)),
                            pl.BlockSpec((D, tn), lambda j: (0, j))],
                  out_specs=[pl.BlockSpec((D, tn), lambda j: (0, j)), _full((16, D))],
                  out_shape=[jax.ShapeDtypeStruct((D, n), F32),
                             jax.ShapeDtypeStruct((16, D), F32)])(c16, dmod16, w)


def _adam_math(w, g, m, v):
    c1 = 1.0 - ADAM_B1 ** ADAM_STEP
    c2 = 1.0 - ADAM_B2 ** ADAM_STEP
    nm = ADAM_B1 * m + (1.0 - ADAM_B1) * g
    nv = ADAM_B2 * v + (1.0 - ADAM_B2) * (g * g)
    return -ADAM_LR * ((nm / c1) / (jnp.sqrt(nv / c2) + ADAM_EPS) + ADAM_WD * w), nm, nv


def _adamw_small(ws, gs, ms, vs):
    n = len(ws)

    def body(*refs):
        for u in range(n):
            d_, nm, nv = _adam_math(refs[u][...], refs[n + u][...], refs[2 * n + u][...], refs[3 * n + u][...])
            refs[4 * n + u][...] = d_
            refs[5 * n + u][...] = nm
            refs[6 * n + u][...] = nv

    specs = [_full(w.shape) for w in ws]
    shapes = [jax.ShapeDtypeStruct(w.shape, F32) for w in ws]
    out = _pcall(body, name="adamw_small", grid=(1,), in_specs=specs * 4, out_specs=specs * 3,
                 out_shape=shapes * 3)(*ws, *gs, *ms, *vs)
    return out[:n], out[n:2 * n], out[2 * n:]


def _cctx_grad(parts, c_ctx):
    def body(p_ref, c_ref, o_ref):
        acc = p_ref[0:1, :]
        for k in range(1, 4):
            acc = acc + p_ref[k:k + 1, :]
        cc = c_ref[...]
        s = _sig(cc)
        o_ref[...] = acc * (s * (1.0 + cc * (1.0 - s)))

    return _pcall(body, name="cctx_grad", grid=(1,), in_specs=[_full(parts.shape), _full((1, D))],
                  out_specs=_full((1, D)), out_shape=jax.ShapeDtypeStruct((1, D), F32))(parts, c_ctx)


ADAM_STEPS = 8


def _adamw_multi(ws, gs, ms, vs, *, name, carry=None):
    n = len(ws)

    def body(*refs):
        for u in range(n):
            refs[4 * n + u][...], refs[5 * n + u][...], refs[6 * n + u][...] = _adam_math(
                refs[u][...], refs[n + u][...], refs[2 * n + u][...], refs[3 * n + u][...])

    specs = [pl.BlockSpec((w.shape[0] // ADAM_STEPS, w.shape[1]), lambda i: (i, 0)) for w in ws]
    shapes = [jax.ShapeDtypeStruct(w.shape, F32) for w in ws]
    res = _pcall(body, name=name, grid=(ADAM_STEPS,), in_specs=specs * 4, out_specs=specs * 3,
                 out_shape=shapes * 3, carry=carry)(*ws, *gs, *ms, *vs)
    out, extra = res if carry is not None else (res, None)
    return (out[:n], out[n:2 * n], out[2 * n:]), extra


def kernel(x, c, ctx, c_ctx, w_ada, b_ada, norm_mix_w, norm_ffn_w, w_in, hgrn_lb_logits, hgrn_norm_w, q_norm_w, k_norm_w, attn_sinks, w_branch_hgrn, w_branch_attn, w_out, w_ffn_gate, w_ffn_up, w_ffn_down, loss_target, m_c_ctx, m_w_ada, m_b_ada, m_norm_mix_w, m_norm_ffn_w, m_w_in, m_hgrn_lb_logits, m_hgrn_norm_w, m_q_norm_w, m_k_norm_w, m_attn_sinks, m_w_branch_hgrn, m_w_branch_attn, m_w_out, m_w_ffn_gate, m_w_ffn_up, m_w_ffn_down, v_c_ctx, v_w_ada, v_b_ada, v_norm_mix_w, v_norm_ffn_w, v_w_in, v_hgrn_lb_logits, v_hgrn_norm_w, v_q_norm_w, v_k_norm_w, v_attn_sinks, v_w_branch_hgrn, v_w_branch_attn, v_w_out, v_w_ffn_gate, v_w_ffn_up, v_w_ffn_down):
    xi, yi, ci = _place()
    chip = 2 * xi + yi
    dev = 2 * chip + ci
    s_len = x.shape[1]

    shards = [w_in[0].T, w_branch_hgrn[0], w_branch_attn[0], w_out[0], w_ffn_gate[0].T, w_ffn_up[0].T,
              w_ffn_down[0]]
    bufs = _cast_place(shards, ci, dev)

    lbrow = jnp.pad(hgrn_lb_logits.reshape(1, 512), ((0, 0), (0, D - 512)))
    blk = jnp.concatenate([c, lbrow, jnp.zeros((6, D), F32)], axis=0)
    nada = w_ada.shape[2]
    b_sh = lax.dynamic_slice(b_ada, (0, chip * nada), (1, nada))
    g0, c16, g1, in8 = _prologue(blk, c_ctx[None], w_ada[0], b_sh, bufs[0])
    lg = g0[0::2, 1, :512].reshape(4, 2, 2, 128).transpose(1, 2, 0, 3).reshape(2, 2, HGW)
    modall = g1[0::2].transpose(1, 0, 2).reshape(16, 4 * nada)
    mod = lax.dynamic_slice(modall, (dev, 0), (1, 6 * D)).reshape(6, D)
    modc = modall[8].reshape(6, D)[:2]

    sq, gx, _, small, rs = _local_step(
        x[0], ctx[0], loss_target[0], mod, modc, norm_mix_w, norm_ffn_w, lg, hgrn_norm_w, q_norm_w,
        k_norm_w, attn_sinks[0], in8.reshape(NCOL, D), bufs[1:], dist=(ci, chip))

    def whole(r):
        return r.reshape(2 * r.shape[1], r.shape[2])

    g_dn, g_g, g_u = [whole(r) for r in rs["ffn_done"]]
    g_bh, g_ba, g_o = [whole(r) for r in rs["mix_done"]]
    in_pairs = rs["in_pairs"]
    rest_rows = (in_pairs[0].shape[1] // 2, in_pairs[0].shape[1] // 2)

    g2, tot = _ag_small(small["raw"], sq)
    loss = 0.5 * jnp.sum(tot[24]) / D
    dmodc_tot = jnp.pad(tot[6:8].reshape(1, 2 * D), ((0, 0), (0, 4 * D)))
    g_b_ada = tot[0:6].reshape(1, 6 * D) + dmodc_tot
    dmod16 = jnp.concatenate([g2[:, 0:6].reshape(8, 6 * D), dmodc_tot, jnp.zeros((7, 6 * D), F32)], axis=0)
    g_w_ada, gc_part = _ada_bwd(c16, lax.dynamic_slice(dmod16, (0, chip * nada), (16, nada)), w_ada[0])
    g3, = _allgather([gc_part[8:16]], name="ag_cctx", in_vmem=True)
    g_c_ctx = _cctx_grad(g3[0::2, 0], c_ctx[None])[0]
    g_nw1 = tot[8:9]
    g_nw2 = tot[9:10]
    g_hw = tot[10, :HGW].reshape(4, HGD).sum(0, keepdims=True)
    g_qnw = tot[10, HGW:].reshape(8, HDIM).sum(0, keepdims=True)
    g_knw = tot[11, :128].reshape(2, HDIM).sum(0, keepdims=True)
    g_sinks = tot[16:24, 0][None]
    g_lg = lax.dynamic_slice(tot[12:16, :HGW].reshape(2, 2, HGW), (0, 0, chip * 128), (2, 2, 128))

    names = ["c_ctx", "w_ada", "b_ada", "norm_mix_w", "norm_ffn_w", "w_in", "hgrn_lb_logits", "hgrn_norm_w",
             "q_norm_w", "k_norm_w", "attn_sinks", "w_branch_hgrn", "w_branch_attn", "w_out", "w_ffn_gate",
             "w_ffn_up", "w_ffn_down"]
    ws = dict(zip(names, [c_ctx, w_ada, b_ada, norm_mix_w, norm_ffn_w, w_in, hgrn_lb_logits, hgrn_norm_w,
                          q_norm_w, k_norm_w, attn_sinks, w_branch_hgrn, w_branch_attn, w_out, w_ffn_gate,
                          w_ffn_up, w_ffn_down]))
    ms = dict(zip(names, [m_c_ctx, m_w_ada, m_b_ada, m_norm_mix_w, m_norm_ffn_w, m_w_in, m_hgrn_lb_logits,
                          m_hgrn_norm_w, m_q_norm_w, m_k_norm_w, m_attn_sinks, m_w_branch_hgrn,
                          m_w_branch_attn, m_w_out, m_w_ffn_gate, m_w_ffn_up, m_w_ffn_down]))
    vs = dict(zip(names, [v_c_ctx, v_w_ada, v_b_ada, v_norm_mix_w, v_norm_ffn_w, v_w_in, v_hgrn_lb_logits,
                          v_hgrn_norm_w, v_q_norm_w, v_k_norm_w, v_attn_sinks, v_w_branch_hgrn,
                          v_w_branch_attn, v_w_out, v_w_ffn_gate, v_w_ffn_up, v_w_ffn_down]))
    transposed = ("w_in", "w_ffn_gate", "w_ffn_up")

    def view(a, n):
        return a[0].T if n in transposed else a[0]

    def unview(a, n):
        return a.T[None] if n in transposed else a[None]

    delta, new_m, new_v, grads = {}, {}, {}, {}

    def big_adamw(group, gs, name, carry=None):
        (d_, m_, v_), extra = _adamw_multi([view(ws[n], n) for n in group], gs, [view(ms[n], n) for n in group],
                                           [view(vs[n], n) for n in group], name=name, carry=carry)
        for i, n in enumerate(group):
            grads[n], delta[n], new_m[n], new_v[n] = (unview(gs[i], n), unview(d_[i], n), unview(m_[i], n),
                                                      unview(v_[i], n))
        return extra

    in_contribs = big_adamw(["w_ffn_down", "w_ffn_gate", "w_ffn_up", "w_out", "w_branch_hgrn", "w_branch_attn"],
                            [g_dn, g_g, g_u, g_o, g_bh, g_ba], "adamw_first",
                            carry=_carry_chipx(in_pairs, rows=rest_rows, into=rs["in_part"]))
    in_reds = _rs_chip_add(in_pairs, in_contribs, ci, chip)
    g_in, = [whole(r) for r in _rs_sibling_gather(in_reds)]
    big_adamw(["w_in", "w_ada"], [g_in, g_w_ada], "adamw_second")
    grads.update(c_ctx=g_c_ctx, b_ada=g_b_ada, norm_mix_w=g_nw1, norm_ffn_w=g_nw2, hgrn_lb_logits=g_lg,
                 hgrn_norm_w=g_hw, q_norm_w=g_qnw, k_norm_w=g_knw, attn_sinks=g_sinks)
    small_names = [n for n in names if n not in delta]

    def two_d(a):
        return a.reshape(1, -1) if a.ndim == 1 else a

    sd, sm_, sv = _adamw_small(*[[two_d(d[n]) for n in small_names] for d in (ws, grads, ms, vs)])
    for i, n in enumerate(small_names):
        for dst, src in ((delta, sd), (new_m, sm_), (new_v, sv)):
            dst[n] = src[i].reshape(ws[n].shape)
    return (loss, gx[None], *[grads[n] for n in names], *[delta[n] for n in names],
            *[new_m[n] for n in names], *[new_v[n] for n in names])
```

```python
import functools

import numpy as np
import jax
import jax.numpy as jnp
from jax import lax
from jax.experimental import pallas as pl
from jax.experimental.pallas import tpu as pltpu

F32 = jnp.float32
BF16 = jnp.bfloat16
HI = lax.Precision.HIGHEST
MESH = pl.DeviceIdType.MESH

D = 1024
L = 256
TM = 256
HGW = 512
HGD = 128
CH = 32
ATW = 512
HDIM = 64
BLK = 128
GRID_W = 64
DFF = 2816
NCOL = 5376
EPS = 1e-6
ROPE_THETA = 10000.0

C_FB, C_INP, C_QHG, C_FF = 0, 1, 2, 3
C_GATES = 1
C_GHG, C_QRAW = 8, 9
C_KV = 20
C_QKV = 6

ADAM_LR, ADAM_B1, ADAM_B2, ADAM_EPS, ADAM_WD, ADAM_STEP = 0.001, 0.9, 0.999, 1e-08, 0.01, 10

NN = (((1,), (0,)), ((), ()))
NT = (((1,), (1,)), ((), ()))
TN = (((0,), (0,)), ((), ()))


def _dot(a, b, dims=NN, prec=None):
    return lax.dot_general(a, b, dims, precision=prec, preferred_element_type=F32)


def _bdot(a, b, dims=NN):
    return _dot(a.astype(BF16), b.astype(BF16), dims)


def _sig(x):
    return 1.0 / (1.0 + jnp.exp(-x))


class _Carry:
    def __init__(self, ins, outs, aliases, scratch, phases):
        self.ins, self.outs, self.aliases, self.scratch, self.phases = ins, outs, aliases, scratch, phases


def _in_hbm(args):
    return [pltpu.with_memory_space_constraint(a, pltpu.HBM) for a in args]


def _out_hbm(shapes):
    if isinstance(shapes, (list, tuple)):
        return [pltpu.HBM(s.shape, s.dtype) for s in shapes]
    return pltpu.HBM(shapes.shape, shapes.dtype)


def _carry_join(a, b):
    na_in, na_out, na_sc = len(a.ins), len(a.outs), len(a.scratch)
    aliases = dict(a.aliases)
    aliases.update({na_in + i: na_out + o for i, o in b.aliases.items()})

    def phases(ins, outs, sems):
        pa = a.phases(ins[:na_in], outs[:na_out], sems[:na_sc])
        pb = b.phases(ins[na_in:], outs[na_out:], sems[na_sc:])

        def both(fa, fb):
            if fa is None and fb is None:
                return None

            def run():
                for fn in (fa, fb):
                    if fn is not None:
                        fn()
            return run

        return tuple(both(fa, fb) for fa, fb in zip(pa, pb))

    return _Carry(list(a.ins) + list(b.ins), list(a.outs) + list(b.outs), aliases,
                  list(a.scratch) + list(b.scratch), phases)


def _pcall(body, *, name, grid, in_specs, out_specs, out_shape, scratch=(), aliases=None, vmem_mb=48,
           carry=None):
    params = pltpu.CompilerParams(dimension_semantics=("arbitrary",) * len(grid),
                                  vmem_limit_bytes=vmem_mb << 20)
    if carry is None:
        plain = pl.pallas_call(
            body, name=name, grid=grid, in_specs=in_specs, out_specs=out_specs, out_shape=_out_hbm(out_shape),
            scratch_shapes=list(scratch), input_output_aliases=aliases or {}, compiler_params=params)
        return lambda *args: plain(*_in_hbm(args))
    single = not isinstance(out_shape, (list, tuple))
    out_specs_l = [out_specs] if single else list(out_specs)
    out_shape_l = [out_shape] if single else list(out_shape)
    n_in, n_out, n_sc = len(in_specs), len(out_shape_l), len(scratch)
    k_in, k_out = len(carry.ins), len(carry.outs)
    nsteps = int(np.prod(grid))
    assert nsteps >= 3

    def wrapped(*refs):
        ins, cins = refs[:n_in], refs[n_in:n_in + k_in]
        o0 = n_in + k_in
        outs, couts = refs[o0:o0 + n_out], refs[o0 + n_out:o0 + n_out + k_out]
        s0 = o0 + n_out + k_out
        sc, csc = refs[s0:s0 + n_sc], refs[s0 + n_sc:]
        step = pl.program_id(0)
        for ax in range(1, len(grid)):
            step = step * grid[ax] + pl.program_id(ax)
        start, mid, end = carry.phases(cins, couts, csc)
        pl.when(step == 0)(start)
        body(*ins, *outs, *sc)
        if mid is not None:
            pl.when(step == nsteps - 2)(mid)
        pl.when(step == nsteps - 1)(end)

    all_aliases = dict(aliases or {})
    all_aliases.update({n_in + i: n_out + o for i, o in carry.aliases.items()})
    call = pl.pallas_call(
        wrapped, name=name, grid=grid, in_specs=list(in_specs) + [ANY] * k_in,
        out_specs=out_specs_l + [ANY] * k_out, out_shape=_out_hbm(out_shape_l + list(carry.outs)),
        scratch_shapes=list(scratch) + list(carry.scratch), input_output_aliases=all_aliases,
        compiler_params=params)

    def run(*args):
        res = call(*_in_hbm(args), *carry.ins)
        core = res[:n_out]
        return (core[0] if single else list(core)), list(res[n_out:])

    return run


def _full(shape):
    nd = len(shape)
    return pl.BlockSpec(shape, lambda *_: (0,) * nd)


ANY = pl.BlockSpec(memory_space=pl.ANY)


def _mm(a, b, *, name, mode="nn", out_dtype=F32, tm, tn, tk):
    if mode == "nn":
        (m, k), (k2, n) = a.shape, b.shape
    elif mode == "nt":
        (m, k), (n, k2) = a.shape, b.shape
    else:
        (k, m), (k2, n) = a.shape, b.shape
    assert k == k2 and m % tm == 0 and n % tn == 0 and k % tk == 0, (name, a.shape, b.shape)
    nk = k // tk
    dims = {"nn": NN, "nt": NT, "tn": TN}[mode]

    def body(a_ref, b_ref, o_ref, acc):
        kk = pl.program_id(2)

        @pl.when(kk == 0)
        def _():
            acc[...] = jnp.zeros_like(acc)

        acc[...] += _bdot(a_ref[...], b_ref[...], dims)

        @pl.when(kk == nk - 1)
        def _():
            o_ref[...] = acc[...].astype(out_dtype)

    a_spec = (pl.BlockSpec((tk, tm), lambda i, j, kk: (kk, i)) if mode == "tn"
              else pl.BlockSpec((tm, tk), lambda i, j, kk: (i, kk)))
    b_spec = (pl.BlockSpec((tn, tk), lambda i, j, kk: (j, kk)) if mode == "nt"
              else pl.BlockSpec((tk, tn), lambda i, j, kk: (kk, j)))
    return _pcall(body, name=name, grid=(m // tm, n // tn, nk), in_specs=[a_spec, b_spec],
                  out_specs=pl.BlockSpec((tm, tn), lambda i, j, kk: (i, j)),
                  out_shape=jax.ShapeDtypeStruct((m, n), out_dtype),
                  scratch=[pltpu.VMEM((tm, tn), F32)])(a, b)


NT_IN = NCOL // 256


def _src_block(j):
    return j + jnp.where(j < 4, 2, jnp.where(j < 6, 3, jnp.where(j < 8, -6, jnp.where(
        j < 16, 5, jnp.where(j < 20, -7, -14)))))


def _mm_in(h, wt, tm, carry=None):
    tt = h.shape[0]

    def body(h_ref, w_ref, o_ref):
        o_ref[...] = _bdot(h_ref[...], w_ref[...], NT)

    return _pcall(body, name="mm_in", grid=(tt // tm, NT_IN),
                  in_specs=[pl.BlockSpec((tm, D), lambda i, j: (i, 0)),
                            pl.BlockSpec((256, D), lambda i, j: (_src_block(j), 0))],
                  out_specs=pl.BlockSpec((tm, 256), lambda i, j: (i, j)),
                  out_shape=jax.ShapeDtypeStruct((tt, NCOL), F32), carry=carry)(h, wt)


def _mm_dh(dp, wt, tm, carry=None):
    tt = dp.shape[0]
    per, ng = 3, NT_IN // 3

    def body(d_ref, w0, w1, w2, o_ref, acc):
        kk = pl.program_id(1)

        @pl.when(kk == 0)
        def _():
            acc[...] = jnp.zeros_like(acc)

        acc[...] += (_bdot(d_ref[:, 0:256], w0[...]) + _bdot(d_ref[:, 256:512], w1[...])
                     + _bdot(d_ref[:, 512:768], w2[...]))

        @pl.when(kk == ng - 1)
        def _():
            o_ref[...] = acc[...]

    wspecs = [pl.BlockSpec((256, D), functools.partial(lambda t, i, kk: (_src_block(per * kk + t), 0), t))
              for t in range(per)]
    return _pcall(body, name="mm_dh", grid=(tt // tm, ng),
                  in_specs=[pl.BlockSpec((tm, per * 256), lambda i, kk: (i, kk))] + wspecs,
                  out_specs=pl.BlockSpec((tm, D), lambda i, kk: (i, 0)),
                  out_shape=jax.ShapeDtypeStruct((tt, D), F32), scratch=[pltpu.VMEM((tm, D), F32)],
                  carry=carry)(dp, wt, wt, wt)


def _mm_gin(dp, h, tk):
    tt = dp.shape[0]
    nk = tt // tk

    def body(d_ref, h_ref, o_ref, acc):
        kk = pl.program_id(1)

        @pl.when(kk == 0)
        def _():
            acc[...] = jnp.zeros_like(acc)

        acc[...] += _bdot(d_ref[...], h_ref[...], TN)

        @pl.when(kk == nk - 1)
        def _():
            o_ref[...] = acc[...].astype(BF16)

    return _pcall(body, name="mm_gin", grid=(NT_IN, nk),
                  in_specs=[pl.BlockSpec((tk, 256), lambda j, kk: (kk, j)),
                            pl.BlockSpec((tk, D), lambda j, kk: (kk, 0))],
                  out_specs=pl.BlockSpec((256, D), lambda j, kk: (_src_block(j), 0)),
                  out_shape=jax.ShapeDtypeStruct((NCOL, D), BF16), scratch=[pltpu.VMEM((256, D), F32)])(dp, h)


def _tok_specs():
    assert L == TM
    return [_full((TM, D)), pl.BlockSpec((TM, D), lambda i: (jnp.maximum(i - 1, 0), 0))]


def _mod1(ctx, x, nw, ss):
    rows = L + x.shape[0]

    def body(c_ref, x_ref, nw_ref, ss_ref, h_ref):
        t = jnp.where(pl.program_id(0) == 0, c_ref[...], x_ref[...])
        r = lax.rsqrt(jnp.mean(t * t, axis=-1, keepdims=True) + EPS)
        s = ss_ref[0]
        h_ref[...] = ((t * r * nw_ref[...]) * (1.0 + s[1:2]) + s[0:1]).astype(BF16)

    return _pcall(body, name="mod1", grid=(rows // TM,),
                  in_specs=_tok_specs() + [_full((1, D)),
                                           pl.BlockSpec((1, 2, D), lambda i: (jnp.minimum(i, 1), 0, 0))],
                  out_specs=pl.BlockSpec((TM, D), lambda i: (i, 0)),
                  out_shape=jax.ShapeDtypeStruct((rows, D), BF16))(ctx, x, nw, ss)


def _norm_bwd_rows(x, dh, nw, scale):
    r = lax.rsqrt(jnp.mean(x * x, axis=-1, keepdims=True) + EPS)
    xh = x * r
    dxh = dh * ((1.0 + scale) * nw)
    dx = r * (dxh - xh * jnp.mean(dxh * xh, axis=-1, keepdims=True))
    return dx, xh


def _out_proj_mod2(mixed, w_o, x, g1, nw2, ss2):
    s_len = x.shape[0]
    tm = 512

    def body(m_ref, w_ref, x_ref, g_ref, nw_ref, ss_ref, ao_ref, x1_ref, h_ref):
        ao = _bdot(m_ref[...], w_ref[...])
        ao_ref[...] = ao
        x1 = x_ref[...] + g_ref[...] * ao
        x1_ref[...] = x1
        r = lax.rsqrt(jnp.mean(x1 * x1, axis=-1, keepdims=True) + EPS)
        s = ss_ref[0]
        h_ref[...] = ((x1 * r * nw_ref[...]) * (1.0 + s[1:2]) + s[0:1]).astype(BF16)

    row = pl.BlockSpec((tm, D), lambda i: (i, 0))
    f = jax.ShapeDtypeStruct((s_len, D), F32)
    return _pcall(body, name="out_proj_mod2", grid=(s_len // tm,),
                  in_specs=[row, _full((D, D)), row, _full((1, D)), _full((1, D)), _full((1, 2, D))],
                  out_specs=[row, row, row],
                  out_shape=[f, f, jax.ShapeDtypeStruct((s_len, D), BF16)])(mixed, w_o, x, g1, nw2, ss2)


TS = 1024


def _acc_call(body, *, name, grid, in_specs, out_specs, out_shape, acc_shapes, args, carry=None):
    return _pcall(body, name=name, grid=grid, in_specs=in_specs, out_specs=out_specs, out_shape=out_shape,
                  scratch=[pltpu.VMEM(s, F32) for s in acc_shapes], carry=carry)(*args)


def _mm_cs(a, w4, *, name):
    m, k = a.shape
    _, _, ns = w4.shape

    def body(a_ref, w_ref, o_ref):
        o_ref[...] = _bdot(a_ref[...], w_ref[0])

    return _pcall(body, name=name, grid=(m // TS, 4),
                  in_specs=[pl.BlockSpec((TS, k), lambda i, j: (i, 0)),
                            pl.BlockSpec((1, k, ns), lambda i, j: (j, 0, 0))],
                  out_specs=pl.BlockSpec((TS, ns), lambda i, j: (i, j)),
                  out_shape=jax.ShapeDtypeStruct((m, 4 * ns), F32))(a, w4)


def _mm_cs_nt(a, w4, *, name):
    m = a.shape[0]
    _, k, ns = w4.shape

    def body(a_ref, w_ref, o_ref, acc):
        j = pl.program_id(1)

        @pl.when(j == 0)
        def _():
            acc[...] = jnp.zeros_like(acc)

        acc[...] += _bdot(a_ref[...], w_ref[0], NT)

        @pl.when(j == 3)
        def _():
            o_ref[...] = acc[...]

    return _acc_call(body, name=name, grid=(m // TS, 4),
                     in_specs=[pl.BlockSpec((TS, ns), lambda i, j: (i, j)),
                               pl.BlockSpec((1, k, ns), lambda i, j: (j, 0, 0))],
                     out_specs=pl.BlockSpec((TS, k), lambda i, j: (i, 0)),
                     out_shape=jax.ShapeDtypeStruct((m, k), F32), acc_shapes=[(TS, k)], args=(a, w4))


def _mm_cs_tn(a, b, ns, *, name):
    s_len, k = a.shape
    nk = s_len // TS

    def body(a_ref, b_ref, o_ref, acc):
        t = pl.program_id(1)

        @pl.when(t == 0)
        def _():
            acc[...] = jnp.zeros_like(acc)

        acc[...] += _bdot(a_ref[...], b_ref[...], TN)

        @pl.when(t == nk - 1)
        def _():
            o_ref[0] = acc[...].astype(o_ref.dtype)

    return _acc_call(body, name=name, grid=(4, nk),
                     in_specs=[pl.BlockSpec((TS, k), lambda j, t: (t, 0)),
                               pl.BlockSpec((TS, ns), lambda j, t: (t, j))],
                     out_specs=pl.BlockSpec((1, k, ns), lambda j, t: (j, 0, 0)),
                     out_shape=jax.ShapeDtypeStruct((4, k, ns), BF16), acc_shapes=[(k, ns)], args=(a, b))


def _ffn_up(h2, g4, u4, carry=None):
    s_len = h2.shape[0]
    ns = g4.shape[1]

    def body(h_ref, g_ref, u_ref, a_ref, b_ref, z_ref):
        h = h_ref[...]
        a = _bdot(h, g_ref[0], NT)
        b = _bdot(h, u_ref[0], NT)
        a_ref[0] = a.astype(BF16)
        b_ref[0] = b.astype(BF16)
        z_ref[0] = (a * _sig(a) * b).astype(BF16)

    w = pl.BlockSpec((1, ns, D), lambda i, j: (j, 0, 0))
    o = pl.BlockSpec((1, TS, ns), lambda i, j: (j, i, 0))
    f = jax.ShapeDtypeStruct((4, s_len, ns), BF16)
    return _pcall(body, name="ffn_up", grid=(s_len // TS, 4),
                  in_specs=[pl.BlockSpec((TS, D), lambda i, j: (i, 0)), w, w], out_specs=[o, o, o],
                  out_shape=[f, f, jax.ShapeDtypeStruct((4, s_len, ns), BF16)], carry=carry)(h2, g4, u4)


def _ffn_down_loss(z4, dn4, x1, g2, tgt):
    _, s_len, ns = z4.shape

    def body(z_ref, w_ref, x1_ref, g_ref, t_ref, sq_ref, dx2_ref, dyb_ref, dg_ref, acc):
        i, j = pl.program_id(0), pl.program_id(1)

        @pl.when((i == 0) & (j == 0))
        def _():
            sq_ref[...] = jnp.zeros_like(sq_ref)
            dg_ref[...] = jnp.zeros_like(dg_ref)

        @pl.when(j == 0)
        def _():
            acc[...] = jnp.zeros_like(acc)

        acc[...] += _bdot(z_ref[0], w_ref[0])

        @pl.when(j == 3)
        def _():
            y_ = acc[...]
            g = g_ref[...]
            e = x1_ref[...] + g * y_ - t_ref[...]
            sq_ref[...] += jnp.sum(e * e, axis=0, keepdims=True)
            dx2 = e * (1.0 / D)
            dx2_ref[...] = dx2
            dyb_ref[...] = (g * dx2).astype(BF16)
            dg_ref[...] += jnp.sum(dx2 * y_, axis=0, keepdims=True)

    row = pl.BlockSpec((TS, D), lambda i, j: (i, 0))
    vec = _full((1, D))
    return _acc_call(body, name="ffn_down_loss", grid=(s_len // TS, 4),
                     in_specs=[pl.BlockSpec((1, TS, ns), lambda i, j: (j, i, 0)),
                               pl.BlockSpec((1, ns, D), lambda i, j: (j, 0, 0)), row, vec, row],
                     out_specs=[vec, row, row, vec],
                     out_shape=[jax.ShapeDtypeStruct((1, D), F32), jax.ShapeDtypeStruct((s_len, D), F32),
                                jax.ShapeDtypeStruct((s_len, D), BF16), jax.ShapeDtypeStruct((1, D), F32)],
                     acc_shapes=[(TS, D)], args=(z4, dn4, x1, g2, tgt))


def _ffn_dz(dyb, dn4, a4, b4):
    _, s_len, ns = a4.shape

    def body(dy_ref, w_ref, a_ref, b_ref, da_ref, db_ref):
        dz = _bdot(dy_ref[...], w_ref[0], NT)
        a = a_ref[0].astype(F32)
        s = _sig(a)
        da_ref[0] = (dz * b_ref[0].astype(F32) * (s * (1.0 + a * (1.0 - s)))).astype(BF16)
        db_ref[0] = (dz * (a * s)).astype(BF16)

    t = pl.BlockSpec((1, TS, ns), lambda i, j: (j, i, 0))
    o = jax.ShapeDtypeStruct((4, s_len, ns), BF16)
    return _pcall(body, name="ffn_dz", grid=(s_len // TS, 4),
                  in_specs=[pl.BlockSpec((TS, D), lambda i, j: (i, 0)),
                            pl.BlockSpec((1, ns, D), lambda i, j: (j, 0, 0)), t, t],
                  out_specs=[t, t], out_shape=[o, o])(dyb, dn4, a4, b4)


def _ffn_gdn(z4, dyb):
    _, s_len, ns = z4.shape
    nk = s_len // TS

    def body(z_ref, dy_ref, o_ref, acc):
        t = pl.program_id(1)

        @pl.when(t == 0)
        def _():
            acc[...] = jnp.zeros_like(acc)

        acc[...] += _bdot(z_ref[0], dy_ref[...], TN)

        @pl.when(t == nk - 1)
        def _():
            o_ref[0] = acc[...].astype(o_ref.dtype)

    return _acc_call(body, name="ffn_gdn", grid=(4, nk),
                     in_specs=[pl.BlockSpec((1, TS, ns), lambda j, t: (j, t, 0)),
                               pl.BlockSpec((TS, D), lambda j, t: (t, 0))],
                     out_specs=pl.BlockSpec((1, ns, D), lambda j, t: (j, 0, 0)),
                     out_shape=jax.ShapeDtypeStruct((4, ns, D), BF16), acc_shapes=[(ns, D)], args=(z4, dyb))


def _ffn_dh2(da4, db4, g4, u4, carry=None):
    _, s_len, ns = da4.shape

    def body(da_ref, db_ref, g_ref, u_ref, o_ref, acc):
        j = pl.program_id(1)

        @pl.when(j == 0)
        def _():
            acc[...] = jnp.zeros_like(acc)

        acc[...] += _bdot(da_ref[0], g_ref[0]) + _bdot(db_ref[0], u_ref[0])

        @pl.when(j == 3)
        def _():
            o_ref[...] = acc[...]

    t = pl.BlockSpec((1, TS, ns), lambda i, j: (j, i, 0))
    w = pl.BlockSpec((1, ns, D), lambda i, j: (j, 0, 0))
    return _acc_call(body, name="ffn_dh2", grid=(s_len // TS, 4), in_specs=[t, t, w, w],
                     out_specs=pl.BlockSpec((TS, D), lambda i, j: (i, 0)),
                     out_shape=jax.ShapeDtypeStruct((s_len, D), F32), acc_shapes=[(TS, D)],
                     args=(da4, db4, g4, u4), carry=carry)


def _ffn_ggu(h2, da4, db4, carry=None):
    _, s_len, ns = da4.shape
    nk = s_len // TS

    def body(h_ref, da_ref, db_ref, gg_ref, gu_ref, acc_g, acc_u):
        t = pl.program_id(1)

        @pl.when(t == 0)
        def _():
            acc_g[...] = jnp.zeros_like(acc_g)
            acc_u[...] = jnp.zeros_like(acc_u)

        h = h_ref[...]
        acc_g[...] += _bdot(da_ref[0], h, TN)
        acc_u[...] += _bdot(db_ref[0], h, TN)

        @pl.when(t == nk - 1)
        def _():
            gg_ref[0] = acc_g[...].astype(BF16)
            gu_ref[0] = acc_u[...].astype(BF16)

    d = pl.BlockSpec((1, TS, ns), lambda j, t: (j, t, 0))
    o = pl.BlockSpec((1, ns, D), lambda j, t: (j, 0, 0))
    f = jax.ShapeDtypeStruct((4, ns, D), BF16)
    return _acc_call(body, name="ffn_ggu", grid=(4, nk),
                     in_specs=[pl.BlockSpec((TS, D), lambda j, t: (t, 0)), d, d], out_specs=[o, o],
                     out_shape=[f, f], acc_shapes=[(ns, D), (ns, D)], args=(h2, da4, db4), carry=carry)


def _mod2_bwd(x1, dh2, dx2, ao, nw2, ss2, g1):
    s_len = x1.shape[0]

    def body(x1_ref, dh_ref, dx2_ref, ao_ref, nw_ref, ss_ref, g_ref,
             dx1_ref, da_ref, dss_ref, dnw_ref, dg_ref):
        i = pl.program_id(0)

        @pl.when(i == 0)
        def _():
            dss_ref[...] = jnp.zeros_like(dss_ref)
            dnw_ref[...] = jnp.zeros_like(dnw_ref)
            dg_ref[...] = jnp.zeros_like(dg_ref)

        dh = dh_ref[...]
        nw = nw_ref[...]
        scale = ss_ref[0][1:2]
        dxn, xh = _norm_bwd_rows(x1_ref[...], dh, nw, scale)
        dx1 = dx2_ref[...] + dxn
        dx1_ref[...] = dx1
        da_ref[...] = (g_ref[...] * dx1).astype(BF16)
        dg_ref[...] += jnp.sum(dx1 * ao_ref[...], axis=0, keepdims=True)
        dsh = jnp.sum(dh, axis=0, keepdims=True)
        dsc = jnp.sum(dh * xh * nw, axis=0, keepdims=True)
        dss_ref[...] += jnp.concatenate([dsh, dsc], axis=0)
        dnw_ref[...] += jnp.sum(dh * xh * (1.0 + scale), axis=0, keepdims=True)

    row = pl.BlockSpec((TM, D), lambda i: (i, 0))
    vec = _full((1, D))
    return _pcall(body, name="mod2_bwd", grid=(s_len // TM,),
                  in_specs=[row, row, row, row, vec, _full((1, 2, D)), vec],
                  out_specs=[row, row, _full((2, D)), vec, vec],
                  out_shape=[jax.ShapeDtypeStruct((s_len, D), F32), jax.ShapeDtypeStruct((s_len, D), BF16),
                             jax.ShapeDtypeStruct((2, D), F32), jax.ShapeDtypeStruct((1, D), F32),
                             jax.ShapeDtypeStruct((1, D), F32)])(x1, dh2, dx2, ao, nw2, ss2, g1)


def _mod1_bwd(ctx, x, dh, dx1, nw1, ss1, carry=None):
    s_len = dx1.shape[0]
    tt = L + s_len

    def body(c_ref, x_ref, dh_ref, dx1_ref, nw_ref, ss_ref, dx_ref, dss_ref, dnw_ref):
        i = pl.program_id(0)
        tok = jnp.where(i == 0, c_ref[...], x_ref[...])

        @pl.when(i == 0)
        def _():
            dnw_ref[...] = jnp.zeros_like(dnw_ref)

        @pl.when(i <= 1)
        def _():
            dss_ref[...] = jnp.zeros_like(dss_ref)

        dh_ = dh_ref[...]
        nw = nw_ref[...]
        scale = ss_ref[0][1:2]
        dxn, xh = _norm_bwd_rows(tok, dh_, nw, scale)

        @pl.when(i >= 1)
        def _():
            dx_ref[...] = dx1_ref[...] + dxn

        dsh = jnp.sum(dh_, axis=0, keepdims=True)
        dsc = jnp.sum(dh_ * xh * nw, axis=0, keepdims=True)
        dss_ref[...] += jnp.concatenate([dsh, dsc], axis=0)[None]
        dnw_ref[...] += jnp.sum(dh_ * xh * (1.0 + scale), axis=0, keepdims=True)

    row = pl.BlockSpec((TM, D), lambda i: (i, 0))
    lat = pl.BlockSpec((TM, D), lambda i: (jnp.maximum(i - 1, 0), 0))
    sel = pl.BlockSpec((1, 2, D), lambda i: (jnp.minimum(i, 1), 0, 0))
    return _pcall(body, name="mod1_bwd", grid=(tt // TM,),
                  in_specs=_tok_specs() + [row, lat, _full((1, D)), sel],
                  out_specs=[lat, sel, _full((1, D))],
                  out_shape=[jax.ShapeDtypeStruct((s_len, D), F32), jax.ShapeDtypeStruct((2, 2, D), F32),
                             jax.ShapeDtypeStruct((1, D), F32)], carry=carry)(ctx, x, dh, dx1, nw1, ss1)


def _rows(c):
    return slice(c * CH, (c + 1) * CH)


def _chunk_masks(rev, transpose=False):
    r = lax.broadcasted_iota(jnp.int32, (TM, TM), 0)
    c = lax.broadcasted_iota(jnp.int32, (TM, TM), 1)
    same = (r // CH) == (c // CH)
    before = (c >= r) if (rev != transpose) else (c <= r)
    return same & before, same


def _chunk_scan(x, rev, transpose=False):
    r = lax.broadcasted_iota(jnp.int32, (CH, CH), 0)
    c = lax.broadcasted_iota(jnp.int32, (CH, CH), 1)
    tri = ((c >= r) if (rev != transpose) else (c <= r)).astype(F32)
    return jnp.concatenate([_dot(tri, x[_rows(ch)], prec=HI) for ch in range(x.shape[0] // CH)], axis=0)


def _chunk_total(x):
    return jnp.concatenate([jnp.broadcast_to(jnp.sum(x[_rows(ch)], axis=0, keepdims=True), (CH, x.shape[1]))
                            for ch in range(x.shape[0] // CH)], axis=0)


def _hgrn_gate(fl, qraw, lg):
    lb = 1.0 / (1.0 + jnp.exp(lg[1:2] - lg[0:1]))
    sg = _sig(fl)
    f = lb + (1.0 - lb) * sg
    q = qraw * _sig(qraw) * (HGD ** -0.5)
    return lb, sg, f, q


def _hgrn_fwd(p, lg, *, rev, carry=None, readout=None):
    tt = p.shape[0]
    nt = tt // TM
    ncht = TM // CH
    d = 1 if rev else 0

    def tile_of(s):
        return jnp.where(s == 0, 0, nt - s) if rev else s

    def body(*refs):
        if readout is None:
            f_ref, inp_ref, q_ref, lg_ref, o_ref, st_ref, state = refs
        else:
            f_ref, inp_ref, q_ref, lg_ref, oo_ref, g_ref, hw_ref, o_ref, st_ref, y_ref, state = refs
        s = pl.program_id(0)

        @pl.when(s == 0)
        def _():
            state[...] = jnp.zeros_like(state)

        _, _, f, q = _hgrn_gate(f_ref[...], q_ref[...], lg_ref[0])
        lf = jnp.log(f)
        causal, _ = _chunk_masks(rev)
        cum = _chunk_scan(lf, rev)
        tot = _chunk_total(lf)
        qd = (q * jnp.exp(cum)).astype(BF16)
        kd = ((1.0 - f) * jnp.exp(-cum)).astype(BF16)
        ke = ((1.0 - f) * jnp.exp(tot - cum)).astype(BF16)
        et = jnp.exp(tot)
        v = inp_ref[...].astype(BF16)
        order = range(ncht - 1, -1, -1) if rev else range(ncht)
        outs = []
        for h in range(4):
            sl = slice(h * HGD, (h + 1) * HGD)
            qd_, kd_, ke_, v_ = qd[:, sl], kd[:, sl], ke[:, sl], v[:, sl]
            pm = jnp.where(causal, _dot(qd_, kd_, NT), 0.0).astype(BF16)
            o_h = _dot(pm, v_)
            upd = [_dot(v_[_rows(c)], ke_[_rows(c)], TN) for c in range(ncht)]
            st = state[h]
            for c in order:
                st_ref[c, h] = st
                st = st * et[c * CH:c * CH + 1, sl] + upd[c]
            state[h] = st
            inter = [_dot(qd_[_rows(c)], st_ref[c, h].astype(BF16), NT) for c in range(ncht)]
            outs.append(o_h + jnp.concatenate(inter, axis=0))
        o_tile = jnp.concatenate(outs, axis=1)
        o_ref[...] = o_tile
        if readout is not None:
            @pl.when(tile_of(s) >= 1)
            def _():
                g = g_ref[...]
                y_ref[...] = (_head_rms(oo_ref[...] + o_tile, None, 4) * hw_ref[...] * (g * _sig(g))).astype(BF16)

    def col(cb):
        return pl.BlockSpec((TM, HGW), lambda s: (tile_of(s), cb))

    in_specs = [col(C_FB if rev else C_FF), col(C_INP), col(C_QHG), pl.BlockSpec((1, 2, HGW), lambda s: (d, 0, 0))]
    out_specs = [col(0), pl.BlockSpec((ncht, 4, HGD, HGD), lambda s: (tile_of(s), 0, 0, 0))]
    out_shape = [jax.ShapeDtypeStruct((tt, HGW), F32), jax.ShapeDtypeStruct((nt * ncht, 4, HGD, HGD), F32)]
    args = [p, p, p, lg]
    if readout is not None:
        in_specs += [col(0), col(C_GHG), _full((1, HGW))]
        args += [readout[0], p, readout[1]]
        assert rev
        out_specs.append(pl.BlockSpec((TM, HGW), lambda s: (jnp.where(s == 0, nt - 2, tile_of(s) - 1), 0)))
        out_shape.append(jax.ShapeDtypeStruct((tt - L, HGW), BF16))
    return _pcall(body, name="hgrn_fwd_rev" if rev else "hgrn_fwd", grid=(nt,), in_specs=in_specs,
                  out_specs=out_specs, out_shape=out_shape, scratch=[pltpu.VMEM((4, HGD, HGD), F32)],
                  carry=carry)(*args)


def _hgrn_bwd(p, lg, do, st, dp, prev, *, rev, carry=None):
    tt = p.shape[0]
    nt = tt // TM
    ncht = TM // CH
    d = 1 if rev else 0
    second = prev is not None

    def tile_of(s):
        return jnp.where(s == nt - 1, 0, s + 1) if rev else nt - 1 - s

    def body(*refs):
        if second:
            (f_ref, inp_ref, q_ref, lg_ref, do_ref, st_ref, dvp_ref, dqp_ref, _dp_in,
             dp_ref, dlg_ref, dstate) = refs
        else:
            (f_ref, inp_ref, q_ref, lg_ref, do_ref, st_ref, _dp_in,
             dp_ref, dv_ref, dq_ref, dlg_ref, dstate) = refs
        s = pl.program_id(0)
        tile = tile_of(s)

        @pl.when(s == 0)
        def _():
            dstate[...] = jnp.zeros_like(dstate)
            dlg_ref[...] = jnp.zeros_like(dlg_ref)

        qraw = q_ref[...]
        lb, sg, f, q = _hgrn_gate(f_ref[...], qraw, lg_ref[0])
        lf = jnp.log(f)
        causal, _ = _chunk_masks(rev)
        causal_t, _ = _chunk_masks(rev, transpose=True)
        cum = _chunk_scan(lf, rev)
        tot = _chunk_total(lf)
        ea, eb, ee, et = jnp.exp(cum), jnp.exp(-cum), jnp.exp(tot - cum), jnp.exp(tot)
        qdf, kdf, kef = q * ea, (1.0 - f) * eb, (1.0 - f) * ee
        qd, kd, ke = qdf.astype(BF16), kdf.astype(BF16), kef.astype(BF16)
        v = inp_ref[...].astype(BF16)
        dob = jnp.where(tile == 0, 0.0, do_ref[...]).astype(BF16)
        order = range(ncht) if rev else range(ncht - 1, -1, -1)
        dq_l, dk_l, dv_l, dcum_l, dtot_l = [], [], [], [], []
        for h in range(4):
            sl = slice(h * HGD, (h + 1) * HGD)
            qd_, kd_, ke_, v_, do_ = qd[:, sl], kd[:, sl], ke[:, sl], v[:, sl], dob[:, sl]
            pmt = jnp.where(causal_t, _dot(kd_, qd_, NT), 0.0).astype(BF16)
            dpm = jnp.where(causal, _dot(do_, v_, NT), 0.0).astype(BF16)
            dpmt = jnp.where(causal_t, _dot(v_, do_, NT), 0.0).astype(BF16)
            dv = _dot(pmt, do_)
            dqd = _dot(dpm, kd_)
            dkd = _dot(dpmt, qd_)
            upd = [_dot(do_[_rows(c)], qd_[_rows(c)], TN) for c in range(ncht)]
            ds = dstate[h]
            ds1 = [None] * ncht
            for c in order:
                ds1[c] = ds
                ds = ds * et[c * CH:c * CH + 1, sl] + upd[c]
            dstate[h] = ds
            dke_c, dv_c, dqd_c, dtot_c = [], [], [], []
            for c in range(ncht):
                st0 = st_ref[c, h]
                dsb = ds1[c].astype(BF16)
                dke_ = _dot(v_[_rows(c)], dsb)
                dke_c.append(dke_)
                dv_c.append(_dot(ke_[_rows(c)], dsb, NT))
                dqd_c.append(_dot(do_[_rows(c)], st0.astype(BF16)))
                dt = (jnp.sum(ds1[c] * st0, axis=0, keepdims=True) * et[c * CH:c * CH + 1, sl]
                      + jnp.sum(dke_ * kef[_rows(c), sl], axis=0, keepdims=True))
                dtot_c.append(jnp.broadcast_to(dt, (CH, HGD)))
            dke = jnp.concatenate(dke_c, axis=0)
            dqd = dqd + jnp.concatenate(dqd_c, axis=0)
            dv_l.append(dv + jnp.concatenate(dv_c, axis=0))
            dtot_l.append(jnp.concatenate(dtot_c, axis=0))
            dq_l.append(dqd * ea[:, sl])
            dk_l.append(dkd * eb[:, sl] + dke * ee[:, sl])
            dcum_l.append(dqd * qdf[:, sl] - dkd * kdf[:, sl] - dke * kef[:, sl])
        dcum = jnp.concatenate(dcum_l, axis=1)
        dlf = _chunk_scan(dcum, rev, transpose=True) + jnp.concatenate(dtot_l, axis=1)
        dq_t = jnp.concatenate(dq_l, axis=1)
        dv_t = jnp.concatenate(dv_l, axis=1)

        df = dlf / f - jnp.concatenate(dk_l, axis=1)
        dfl = df * (1.0 - lb) * sg * (1.0 - sg)
        dlb = jnp.sum(df * (1.0 - sg), axis=0, keepdims=True)
        dl0 = dlb * lb * (1.0 - lb)
        dlg_ref[...] += jnp.concatenate([dl0, -dl0], axis=0)[None]
        if second:
            sq = _sig(qraw)
            dqr = (dqp_ref[...] + dq_t) * (HGD ** -0.5) * (sq * (1.0 + qraw * (1.0 - sq)))
            dp_ref[...] = jnp.concatenate([dfl, dvp_ref[...] + dv_t, dqr], axis=1).astype(BF16)
        else:
            dp_ref[...] = dfl.astype(BF16)
            dv_ref[...] = dv_t
            dq_ref[...] = dq_t

    def col(cb):
        return pl.BlockSpec((TM, HGW), lambda s: (tile_of(s), cb))

    tok = pl.BlockSpec((TM, HGW), lambda s: (tile_of(s), 0))
    in_specs = [col(C_FB if rev else C_FF), col(C_INP), col(C_QHG),
                pl.BlockSpec((1, 2, HGW), lambda s: (d, 0, 0)),
                pl.BlockSpec((TM, HGW), lambda s: (jnp.maximum(tile_of(s) - 1, 0), 0)),
                pl.BlockSpec((ncht, 4, HGD, HGD), lambda s: (tile_of(s), 0, 0, 0))]
    args = [p, p, p, lg, do, st]
    dlg_spec = _full((1, 2, HGW))
    dlg_shape = jax.ShapeDtypeStruct((1, 2, HGW), F32)
    if second:
        in_specs += [tok, tok]
        args += [prev[0], prev[1]]
        out_specs = [pl.BlockSpec((TM, 3 * HGW), lambda s: (tile_of(s), 0)), dlg_spec]
        out_shape = [jax.ShapeDtypeStruct(dp.shape, BF16), dlg_shape]
    else:
        out_specs = [pl.BlockSpec((TM, HGW), lambda s: (tile_of(s), C_FB if rev else C_FF)), tok, tok, dlg_spec]
        out_shape = [jax.ShapeDtypeStruct(dp.shape, BF16), jax.ShapeDtypeStruct((tt, HGW), F32),
                     jax.ShapeDtypeStruct((tt, HGW), F32), dlg_shape]
    in_specs.append(ANY)
    args.append(dp)
    return _pcall(body, name="hgrn_bwd_rev" if rev else "hgrn_bwd", grid=(nt,),
                  in_specs=in_specs, out_specs=out_specs, out_shape=out_shape,
                  scratch=[pltpu.VMEM((4, HGD, HGD), F32)],
                  aliases={len(args) - 1: 0}, carry=carry)(*args)


def _head_rms(o, w, nheads):
    outs = []
    for h in range(nheads):
        oh = o[:, h * HGD:(h + 1) * HGD]
        outs.append(oh * lax.rsqrt(jnp.mean(oh * oh, axis=-1, keepdims=True) + EPS))
    return jnp.concatenate(outs, axis=1)


def _readout(o0, o1, p, hw4):
    s_len = o0.shape[0] - L

    def body(o0_ref, o1_ref, g_ref, w_ref, y_ref):
        xh = _head_rms(o0_ref[...] + o1_ref[...], None, 4)
        g = g_ref[...]
        y_ref[...] = (xh * w_ref[...] * (g * _sig(g))).astype(BF16)

    lat = pl.BlockSpec((TM, HGW), lambda i: (i + 1, 0))
    return _pcall(body, name="readout", grid=(s_len // TM,),
                  in_specs=[lat, lat, pl.BlockSpec((TM, HGW), lambda i: (i + 1, C_GHG)), _full((1, HGW))],
                  out_specs=pl.BlockSpec((TM, HGW), lambda i: (i, 0)),
                  out_shape=jax.ShapeDtypeStruct((s_len, HGW), BF16))(o0, o1, p, hw4)


def _readout_bwd(o0, o1, p, hw4, dy, dp, carry=None):
    tt = o0.shape[0]
    s_len = tt - L

    def body(o0_ref, o1_ref, g_ref, w_ref, dy_ref, _dp_in, dp_ref, do_ref, dw_ref):
        i = pl.program_id(0)

        @pl.when(i == 0)
        def _():
            dw_ref[...] = jnp.zeros_like(dw_ref)
            dp_ref[...] = jnp.zeros_like(dp_ref)

        @pl.when(i >= 1)
        def _():
            o = o0_ref[...] + o1_ref[...]
            g = g_ref[...]
            w = w_ref[...]
            sg = _sig(g)
            dy_ = dy_ref[...]
            dsw = dy_ * (g * sg)
            outs, xhs = [], []
            for h in range(4):
                sl = slice(h * HGD, (h + 1) * HGD)
                oh = o[:, sl]
                r = lax.rsqrt(jnp.mean(oh * oh, axis=-1, keepdims=True) + EPS)
                xh = oh * r
                dxh = dsw[:, sl] * w[:, sl]
                outs.append(r * (dxh - xh * jnp.mean(dxh * xh, axis=-1, keepdims=True)))
                xhs.append(xh)
            xh = jnp.concatenate(xhs, axis=1)
            do_ref[...] = jnp.concatenate(outs, axis=1)
            dp_ref[...] = (dy_ * xh * w * (sg * (1.0 + g * (1.0 - sg)))).astype(BF16)
            dw_ref[...] += jnp.sum(dsw * xh, axis=0, keepdims=True)

    tok = pl.BlockSpec((TM, HGW), lambda i: (i, 0))
    lat = pl.BlockSpec((TM, HGW), lambda i: (jnp.maximum(i - 1, 0), 0))
    return _pcall(body, name="readout_bwd", grid=(tt // TM,),
                  in_specs=[tok, tok, pl.BlockSpec((TM, HGW), lambda i: (i, C_GHG)), _full((1, HGW)), lat, ANY],
                  out_specs=[pl.BlockSpec((TM, HGW), lambda i: (i, C_GHG)), lat, _full((1, HGW))],
                  out_shape=[jax.ShapeDtypeStruct(dp.shape, BF16), jax.ShapeDtypeStruct((s_len, HGW), F32),
                             jax.ShapeDtypeStruct((1, HGW), F32)],
                  aliases={5: 0}, carry=carry)(o0, o1, p, hw4, dy, dp)


def _rope_tables(s_len):
    t = np.arange(s_len)
    inv = ROPE_THETA ** (-np.arange(0, 32, 2, dtype=np.float64) / 32)
    def half(pos):
        ang = pos[:, None].astype(np.float64) * inv[None, :]
        return (np.concatenate([np.cos(ang), np.cos(ang)], 1), np.concatenate([-np.sin(ang), np.sin(ang)], 1))
    cr, sr = half(t // GRID_W)
    cc, sc = half(t % GRID_W)
    cos = np.concatenate([cr, cc, cr, cc], 1)
    sin = np.concatenate([sr, sc, sr, sc], 1)
    cos = np.concatenate([np.ones((L, 128)), cos], 0)
    sin = np.concatenate([np.zeros((L, 128)), sin], 0)
    return jnp.asarray(cos, F32), jnp.asarray(sin, F32)


def _blockdiag(n, w):
    i = np.arange(n)
    return jnp.asarray((i[:, None] // w == i[None, :] // w) / float(w), F32)


def _dup_matrix():
    m = np.zeros((128, 512), np.float32)
    for g in range(2):
        for j in range(4):
            for dd in range(HDIM):
                m[64 * g + dd, 256 * g + 64 * j + dd] = 1.0
    return m


def _head_mean(x, blockdiag):
    return _dot(x, blockdiag, prec=lax.Precision.HIGH)


def _rot(x):
    n = x.shape[1]
    lane = lax.broadcasted_iota(jnp.int32, x.shape, 1)
    return jnp.where((lane % 32) < 16, pltpu.roll(x, n - 16, 1), pltpu.roll(x, 16, 1))


def _qk_prep(p, cos, sin, qnw8, knw2, bd512, bd128, dup):
    tt = p.shape[0]

    def body(q_ref, kv_ref, cos_ref, sin_ref, qw_ref, kw_ref, b5_ref, b1_ref, dup_ref,
             qr_ref, k4_ref, v4_ref):
        cos_, sin_ = cos_ref[...], sin_ref[...]
        q = q_ref[...]
        qn = q * lax.rsqrt(_head_mean(q * q, b5_ref[...]) + EPS) * qw_ref[...]
        cos4 = jnp.concatenate([cos_] * 4, axis=1)
        sin4 = jnp.concatenate([sin_] * 4, axis=1)
        qr_ref[...] = ((qn * cos4 + _rot(qn) * sin4) * (HDIM ** -0.5)).astype(BF16)
        kv = kv_ref[...]
        k, v = kv[:, :128], kv[:, 128:]
        kn = k * lax.rsqrt(_head_mean(k * k, b1_ref[...]) + EPS) * kw_ref[...]
        kr = kn * cos_ + _rot(kn) * sin_
        k4_ref[...] = _bdot(kr, dup_ref[...]).astype(BF16)
        v4_ref[...] = _bdot(v, dup_ref[...]).astype(BF16)

    row = lambda w, cb: pl.BlockSpec((TM, w), lambda i: (i, cb))
    out = jax.ShapeDtypeStruct((tt, ATW), BF16)
    return _pcall(body, name="qk_prep", grid=(tt // TM,),
                  in_specs=[row(ATW, C_QRAW), row(256, C_KV), row(128, 0), row(128, 0),
                            _full((1, ATW)), _full((1, 128)), _full((ATW, ATW)), _full((128, 128)),
                            _full((128, ATW))],
                  out_specs=[row(ATW, 0)] * 3, out_shape=[out] * 3)(
                      p, p, cos, sin, qnw8, knw2, bd512, bd128, dup)


def _attn_masks(i, nb):
    r = lax.broadcasted_iota(jnp.int32, (4 * BLK, 3 * BLK + L), 0) % BLK
    c = lax.broadcasted_iota(jnp.int32, (4 * BLK, 3 * BLK + L), 1)
    kpos = (i - 1) * BLK + c
    loc = (jnp.abs(c - BLK - r) <= BLK) & (kpos >= 0) & (kpos < nb * BLK)
    return loc | (c >= 3 * BLK)


def _stack_mask():
    r = lax.broadcasted_iota(jnp.int32, (4 * BLK, 256), 0)
    lane = lax.broadcasted_iota(jnp.int32, (4 * BLK, 256), 1)
    return (r // BLK) == (lane // HDIM)


def _stack_heads(xg, fill=0.0):
    x4 = jnp.concatenate([xg] * 4, axis=0)
    return jnp.where(_stack_mask(), x4, jnp.full_like(x4, fill))


def _unstack_heads(x4):
    out = jnp.where(_lane_mask(0), x4[0:BLK], 0.0)
    for j in range(1, 4):
        out = out + jnp.where(_lane_mask(j), x4[j * BLK:(j + 1) * BLK], 0.0)
    return out


def _per_head_rows(vals):
    return jnp.concatenate([jnp.broadcast_to(v, (BLK, 1)) for v in vals], axis=0)


def _lane_mask(j):
    lane = lax.broadcasted_iota(jnp.int32, (1, 256), 1)
    return (lane // HDIM) == j


def _attn_specs(nb):
    blk = lambda off: pl.BlockSpec((BLK, ATW), lambda i: (jnp.clip(i + off, 0, nb - 1) + 2, 0))
    ctx = pl.BlockSpec((L, ATW), lambda i: (0, 0))
    return blk, ctx


def _attn_fwd(qr, k4, v4, sinks, carry=None):
    tt = qr.shape[0]
    s_len = tt - L
    nb = s_len // BLK

    def body(sk_ref, q_ref, kp, ko, kn, kc, vp, vo, vn, vc, y_ref, lse_ref):
        i = pl.program_id(0)
        valid = _attn_masks(i, nb)
        q = q_ref[...]
        ys, lses = [], []
        for g in range(2):
            gs = slice(256 * g, 256 * g + 256)
            kcat = jnp.concatenate([kp[:, gs], ko[:, gs], kn[:, gs], kc[:, gs]], axis=0)
            vcat = jnp.concatenate([vp[:, gs], vo[:, gs], vn[:, gs], vc[:, gs]], axis=0)
            sink4 = _per_head_rows([sk_ref[4 * g + j] for j in range(4)])
            q4 = _stack_heads(q[:, gs])
            o_parts, l_parts = [], []
            for hp in range(2):
                rows = slice(2 * BLK * hp, 2 * BLK * (hp + 1))
                sink = sink4[rows]
                s = jnp.where(valid[rows], _dot(q4[rows], kcat, NT), -1e30)
                m = jnp.maximum(jnp.max(s, axis=-1, keepdims=True), sink)
                e = jnp.exp(s - m)
                den = jnp.sum(e, axis=-1, keepdims=True) + jnp.exp(sink - m)
                o_parts.append(_bdot(e * (1.0 / den), vcat))
                l_parts.append(jnp.broadcast_to(m + jnp.log(den), (2 * BLK, 256)))
            ys.append(_unstack_heads(jnp.concatenate(o_parts, axis=0)))
            lses.append(_unstack_heads(jnp.concatenate(l_parts, axis=0)))
        y_ref[...] = jnp.concatenate(ys, axis=1).astype(BF16)
        lse_ref[...] = jnp.concatenate(lses, axis=1)

    blk, ctx = _attn_specs(nb)
    out = pl.BlockSpec((BLK, ATW), lambda i: (i, 0))
    return _pcall(body, name="attn_fwd", grid=(nb,),
                  in_specs=[pl.BlockSpec(memory_space=pltpu.SMEM), blk(0),
                            blk(-1), blk(0), blk(1), ctx, blk(-1), blk(0), blk(1), ctx],
                  out_specs=[out, out],
                  out_shape=[jax.ShapeDtypeStruct((s_len, ATW), BF16),
                             jax.ShapeDtypeStruct((s_len, ATW), F32)], carry=carry)(
                      sinks, qr, k4, k4, k4, k4, v4, v4, v4, v4)


def _attn_bwd(qr, k4, v4, sinks, y, lse, dy, carry=None):
    tt = qr.shape[0]
    s_len = tt - L
    nb = s_len // BLK

    def body(sk_ref, q_ref, kp, ko, kn, kc, vp, vo, vn, vc, y_ref, lse_ref, dy_ref,
             dq_ref, dkw_ref, dvw_ref, dkc_ref, dvc_ref, dsk_ref):
        i = pl.program_id(0)

        @pl.when(i == 0)
        def _():
            dkc_ref[...] = jnp.zeros_like(dkc_ref)
            dvc_ref[...] = jnp.zeros_like(dvc_ref)
            dsk_ref[...] = jnp.zeros_like(dsk_ref)

        valid = _attn_masks(i, nb)
        q = q_ref[...]
        dy_ = dy_ref[...]
        dly = dy_ * y_ref[...].astype(F32)
        lse_ = lse_ref[...]
        dqs = []
        for g in range(2):
            gs = slice(256 * g, 256 * g + 256)
            kcat = jnp.concatenate([kp[:, gs], ko[:, gs], kn[:, gs], kc[:, gs]], axis=0)
            vcat = jnp.concatenate([vp[:, gs], vo[:, gs], vn[:, gs], vc[:, gs]], axis=0)
            q4 = _stack_heads(q[:, gs])
            dy4 = _stack_heads(dy_[:, gs]).astype(BF16)
            lse4 = jnp.max(_stack_heads(lse_[:, gs], fill=-1e30), axis=-1, keepdims=True)
            delta = jnp.sum(_stack_heads(dly[:, gs]), axis=-1, keepdims=True)
            sink = _per_head_rows([sk_ref[4 * g + j] for j in range(4)])
            pr = jnp.where(valid, jnp.exp(_dot(q4, kcat, NT) - lse4), 0.0)
            dsb = (pr * (_dot(dy4, vcat, NT) - delta)).astype(BF16)
            dsink = jnp.exp(sink - lse4) * delta
            for j in range(4):
                dsk_ref[4 * g + j:4 * g + j + 1, :] += jnp.broadcast_to(
                    -jnp.sum(dsink[j * BLK:(j + 1) * BLK], axis=0, keepdims=True), (1, 128))
            dqs.append(_unstack_heads(_dot(dsb, kcat)))
            dkg = _dot(dsb, q4, TN)
            dvg = _dot(pr.astype(BF16), dy4, TN)
            dkw_ref[0, :, gs] = dkg[:3 * BLK]
            dvw_ref[0, :, gs] = dvg[:3 * BLK]
            dkc_ref[:, gs] += dkg[3 * BLK:]
            dvc_ref[:, gs] += dvg[3 * BLK:]
        dq_ref[...] = jnp.concatenate(dqs, axis=1)

    blk, ctx = _attn_specs(nb)
    out = pl.BlockSpec((BLK, ATW), lambda i: (i, 0))
    win = pl.BlockSpec((1, 3 * BLK, ATW), lambda i: (i, 0, 0))
    acc = _full((L, ATW))
    return _pcall(body, name="attn_bwd", grid=(nb,),
                  in_specs=[pl.BlockSpec(memory_space=pltpu.SMEM), blk(0),
                            blk(-1), blk(0), blk(1), ctx, blk(-1), blk(0), blk(1), ctx, out, out, out],
                  out_specs=[out, win, win, acc, acc, _full((8, 128))],
                  out_shape=[jax.ShapeDtypeStruct((s_len, ATW), F32),
                             jax.ShapeDtypeStruct((nb, 3 * BLK, ATW), F32),
                             jax.ShapeDtypeStruct((nb, 3 * BLK, ATW), F32),
                             jax.ShapeDtypeStruct((L, ATW), F32), jax.ShapeDtypeStruct((L, ATW), F32),
                             jax.ShapeDtypeStruct((8, 128), F32)], carry=carry)(
                      sinks, qr, k4, k4, k4, k4, v4, v4, v4, v4, y, lse, dy)


def _attn_post(p, cos, sin, qnw8, knw2, bd512, bd128, dupt, dq, dkw, dvw, dkc, dvc, dp, carry=None):
    tt = p.shape[0]
    s_len = tt - L
    nb = s_len // BLK
    nctx = L // BLK

    def body(q_ref, kv_ref, cos_ref, sin_ref, qw_ref, kw_ref, b5_ref, b1_ref, dupt_ref,
             dq_ref, kwp, kwo, kwn, vwp, vwo, vwn, dkc_ref, dvc_ref, _dp_in,
             dp_ref, dqw_ref, dkw_ref):
        t = pl.program_id(0)
        j = t - nctx

        @pl.when(t == 0)
        def _():
            dqw_ref[...] = jnp.zeros_like(dqw_ref)
            dkw_ref[...] = jnp.zeros_like(dkw_ref)

        is_lat = t >= nctx
        cos_, sin_ = cos_ref[...], sin_ref[...]
        has_p = is_lat & (j >= 1)
        has_n = is_lat & (j <= nb - 2)
        dk4 = (jnp.where(is_lat, kwo[0], dkc_ref[...]) + jnp.where(has_p, kwp[0], 0.0)
               + jnp.where(has_n, kwn[0], 0.0))
        dv4 = (jnp.where(is_lat, vwo[0], dvc_ref[...]) + jnp.where(has_p, vwp[0], 0.0)
               + jnp.where(has_n, vwn[0], 0.0))
        dkr = _dot(dk4, dupt_ref[...], prec=HI)
        dv = _dot(dv4, dupt_ref[...], prec=HI)
        kv = kv_ref[...]
        k = kv[:, :128]
        kw = kw_ref[...]
        rk = lax.rsqrt(_head_mean(k * k, b1_ref[...]) + EPS)
        xk = k * rk
        dkn = dkr * cos_ + _rot(dkr * sin_)
        dxk = dkn * kw
        dk = rk * (dxk - xk * _head_mean(dxk * xk, b1_ref[...]))
        dkw_ref[...] += jnp.sum(dkn * xk, axis=0, keepdims=True)
        q = q_ref[...]
        qw = qw_ref[...]
        rq = lax.rsqrt(_head_mean(q * q, b5_ref[...]) + EPS)
        xq = q * rq
        cos4 = jnp.concatenate([cos_] * 4, axis=1)
        sin4 = jnp.concatenate([sin_] * 4, axis=1)
        dqr = jnp.where(is_lat, dq_ref[...], 0.0) * (HDIM ** -0.5)
        dqn = dqr * cos4 + _rot(dqr * sin4)
        dxq = dqn * qw
        dqraw = rq * (dxq - xq * _head_mean(dxq * xq, b5_ref[...]))
        dqw_ref[...] += jnp.sum(dqn * xq, axis=0, keepdims=True)
        dp_ref[...] = jnp.concatenate([dqraw, dk, dv], axis=1).astype(BF16)

    row = lambda w, cb: pl.BlockSpec((BLK, w), lambda t: (t, cb))
    lat = pl.BlockSpec((BLK, ATW), lambda t: (jnp.maximum(t - nctx, 0), 0))

    def part(off):
        return pl.BlockSpec((1, BLK, ATW), lambda t: (jnp.clip(t - nctx + off, 0, nb - 1), 1 - off, 0))

    cacc = pl.BlockSpec((BLK, ATW), lambda t: (jnp.minimum(t, nctx - 1), 0))
    return _pcall(body, name="attn_post", grid=(tt // BLK,),
                  in_specs=[row(ATW, C_QRAW), row(256, C_KV), row(128, 0), row(128, 0),
                            _full((1, ATW)), _full((1, 128)), _full((ATW, ATW)), _full((128, 128)),
                            _full((ATW, 128)), lat, part(-1), part(0), part(1), part(-1), part(0), part(1),
                            cacc, cacc, ANY],
                  out_specs=[pl.BlockSpec((BLK, 768), lambda t: (t, C_QKV)), _full((1, ATW)), _full((1, 128))],
                  out_shape=[jax.ShapeDtypeStruct(dp.shape, BF16), jax.ShapeDtypeStruct((1, ATW), F32),
                             jax.ShapeDtypeStruct((1, 128), F32)],
                  aliases={18: 0}, carry=carry)(p, p, cos, sin, qnw8, knw2, bd512, bd128, dupt,
                                   dq, dkw, dkw, dkw, dvw, dvw, dvw, dkc, dvc, dp)


def _branch_merge(y_hg, y_at, bh4, ba4, p):
    s_len = y_hg.shape[0]

    def body(yh_ref, ya_ref, bh_ref, ba_ref, gh_ref, ga_ref, ah_ref, aa_ref, m_ref):
        yh, ya = yh_ref[...], ya_ref[...]
        ah = jnp.concatenate([_bdot(yh, bh_ref[j]) for j in range(4)], axis=1)
        aa = jnp.concatenate([_bdot(ya, ba_ref[j]) for j in range(4)], axis=1)
        ah_ref[...] = ah
        aa_ref[...] = aa
        m_ref[...] = (_sig(gh_ref[...]) * ah + _sig(ga_ref[...]) * aa).astype(BF16)

    row = pl.BlockSpec((TM, D), lambda i: (i, 0))
    y = pl.BlockSpec((TM, HGW), lambda i: (i, 0))
    f = jax.ShapeDtypeStruct((s_len, D), F32)
    return _pcall(body, name="branch_merge", grid=(s_len // TM,),
                  in_specs=[y, y, _full(bh4.shape), _full(ba4.shape),
                            pl.BlockSpec((TM, D), lambda i: (i + 1, 2)), pl.BlockSpec((TM, D), lambda i: (i + 1, 3))],
                  out_specs=[row, row, row],
                  out_shape=[f, f, jax.ShapeDtypeStruct((s_len, D), BF16)])(y_hg, y_at, bh4, ba4, p, p)


def _branch_bwd(dmh, dma, bh4, ba4, y_hg, y_at):
    s_len = dmh.shape[0]
    nk = s_len // TS
    ns = D // 4

    def body(dh_ref, da_ref, bh_ref, ba_ref, yh_ref, ya_ref, dyh_ref, dya_ref, gh_ref, ga_ref, acc_h, acc_a):
        t = pl.program_id(0)

        @pl.when(t == 0)
        def _():
            acc_h[...] = jnp.zeros_like(acc_h)
            acc_a[...] = jnp.zeros_like(acc_a)

        for d_ref, w_ref, y_ref, dy_ref, acc in ((dh_ref, bh_ref, yh_ref, dyh_ref, acc_h),
                                                 (da_ref, ba_ref, ya_ref, dya_ref, acc_a)):
            y = y_ref[...]
            dy = jnp.zeros((TS, HGW), F32)
            for j in range(4):
                dj = d_ref[:, j * ns:(j + 1) * ns]
                dy = dy + _bdot(dj, w_ref[j], NT)
                acc[j] += _bdot(y, dj, TN)
            dy_ref[...] = dy

        @pl.when(t == nk - 1)
        def _():
            gh_ref[...] = acc_h[...].astype(BF16)
            ga_ref[...] = acc_a[...].astype(BF16)

    dm = pl.BlockSpec((TS, D), lambda t: (t, 0))
    y = pl.BlockSpec((TS, HGW), lambda t: (t, 0))
    w = _full(bh4.shape)
    fy = jax.ShapeDtypeStruct((s_len, HGW), F32)
    gw = jax.ShapeDtypeStruct(bh4.shape, BF16)
    return _pcall(body, name="branch_bwd", grid=(nk,), in_specs=[dm, dm, w, w, y, y],
                  out_specs=[y, y, w, w], out_shape=[fy, fy, gw, gw],
                  scratch=[pltpu.VMEM(bh4.shape, F32)] * 2)(dmh, dma, bh4, ba4, y_hg, y_at)


def _merge_bwd(dattn, w_o, mixed, ah, aa, p, carry=None):
    tt = p.shape[0]
    s_len = tt - L
    nt = tt // TM

    def body(da_ref, wo_ref, mx_ref, ah_ref, aa_ref, gh_ref, ga_ref, dp_ref, dmh_ref, dma_ref, go_ref, acc):
        i = pl.program_id(0)

        @pl.when(i == 0)
        def _():
            dp_ref[...] = jnp.zeros_like(dp_ref)
            acc[...] = jnp.zeros_like(acc)

        @pl.when(i >= 1)
        def _():
            da = da_ref[...]
            acc[...] += _bdot(mx_ref[...], da, TN)
            dm_ = _bdot(da, wo_ref[...], NT)
            sh, sa = _sig(gh_ref[...]), _sig(ga_ref[...])
            dp_ref[...] = jnp.concatenate([dm_ * ah_ref[...] * sh * (1.0 - sh),
                                           dm_ * aa_ref[...] * sa * (1.0 - sa)], axis=1).astype(BF16)
            dmh_ref[...] = (dm_ * sh).astype(BF16)
            dma_ref[...] = (dm_ * sa).astype(BF16)

        @pl.when(i == nt - 1)
        def _():
            go_ref[...] = acc[...].astype(BF16)

    lat = pl.BlockSpec((TM, D), lambda i: (jnp.maximum(i - 1, 0), 0))
    return _pcall(body, name="merge_bwd", grid=(nt,),
                  in_specs=[lat, _full((D, D)), lat, lat, lat, pl.BlockSpec((TM, D), lambda i: (i, 2)),
                            pl.BlockSpec((TM, D), lambda i: (i, 3))],
                  out_specs=[pl.BlockSpec((TM, 2 * D), lambda i: (i, C_GATES)), lat, lat, _full((D, D))],
                  out_shape=[jax.ShapeDtypeStruct((tt, NCOL), BF16), jax.ShapeDtypeStruct((s_len, D), BF16),
                             jax.ShapeDtypeStruct((s_len, D), BF16), jax.ShapeDtypeStruct((D, D), BF16)],
                  scratch=[pltpu.VMEM((D, D), F32)], carry=carry)(dattn, w_o, mixed, ah, aa, p, p)


def _local_step(x, ctx, tgt, mod, modc, nw1, nw2, lg, hw, qnw, knw, sinks,
                w_in, wts, dist=None):
    s_len = x.shape[0]
    tt = s_len + L
    ss1 = jnp.stack([modc, mod[0:2]])
    ss2 = mod[3:5][None]
    g1, g2 = mod[2:3], mod[5:6]
    hw4 = jnp.tile(hw, (1, 4))
    qnw8 = jnp.tile(qnw, (1, 8))
    knw2 = jnp.tile(knw, (1, 2))
    cos, sin = _rope_tables(s_len)
    bd512, bd128 = _blockdiag(ATW, HDIM), _blockdiag(128, HDIM)
    dupm = _dup_matrix()
    dup, dupt = jnp.asarray(dupm, BF16), jnp.asarray(dupm.T, F32)
    tmt = tt

    def four(b):
        return b.reshape(4, 2 * b.shape[1], b.shape[2])

    def halves(g):
        return g.reshape(4, 2, g.shape[1] // 2, g.shape[2])

    h = _mod1(ctx, x, nw1, ss1)
    if dist is None:
        bh4, ba4, w_o, g4, u4, dn4 = wts
        p = _mm_in(h, w_in, tmt)
        o0, st0 = _hgrn_fwd(p, lg, rev=False)
        o1, st1, y_hg = _hgrn_fwd(p, lg, rev=True, readout=(o0, hw4))
    else:
        core, chip = dist
        half = wts[3].shape[1] // 2
        p, first = _mm_in(h, w_in, tmt, carry=_carry_join(_carry_gather(list(wts[0:3])),
                                                          _carry_gather([wts[3]], rows=[(0, half)])))
        (o0, st0), (g8,) = _hgrn_fwd(p, lg, rev=False, carry=_carry_gather([first[3]], rows=[(half, half)]))
        (o1, st1, y_hg), (dn8a,) = _hgrn_fwd(p, lg, rev=True, readout=(o0, hw4),
                                             carry=_carry_gather([wts[5]], rows=[(0, half)]))
        bh4, ba4, w_o, g4 = four(first[0]), four(first[1]), four(first[2]).reshape(D, D), four(g8)
    qr, k4, v4 = _qk_prep(p, cos, sin, qnw8, knw2, bd512, bd128, dup)
    if dist is None:
        y_at, lse = _attn_fwd(qr, k4, v4, sinks)
    else:
        (y_at, lse), (u8, dn8) = _attn_fwd(qr, k4, v4, sinks,
                                           carry=_carry_gather([wts[4], dn8a], rows=[None, (half, half)]))
        u4, dn4 = four(u8), four(dn8)
    ah, aa, mixed = _branch_merge(y_hg, y_at, bh4, ba4, p)
    ao, x1, h2 = _out_proj_mod2(mixed, w_o, x, g1, nw2, ss2)
    a4, b4, z4 = _ffn_up(h2, g4, u4)
    sq, dx2, dyb, dg2 = _ffn_down_loss(z4, dn4, x1, g2, tgt)

    da4, db4 = _ffn_dz(dyb, dn4, a4, b4)
    g_dn = _ffn_gdn(z4, dyb)
    if dist is None:
        dh2 = _ffn_dh2(da4, db4, g4, u4)
    else:
        dn_units = [halves(g_dn)]
        dh2, dn_recv = _ffn_dh2(da4, db4, g4, u4, carry=_carry_pairx(dn_units))
        dn_pairs = _rs_pair_add(dn_units, dn_recv, core)
    if dist is None:
        g_g, g_u = _ffn_ggu(h2, da4, db4)
    else:
        (g_g, g_u), c_dn = _ffn_ggu(h2, da4, db4, carry=_carry_chipx(dn_pairs))
        red_dn = _rs_chip_add(dn_pairs, c_dn, core, chip)
    dx1, dattn, dss2, dnw2, dg1 = _mod2_bwd(x1, dh2, dx2, ao, nw2, ss2, g1)
    if dist is None:
        dp, dmh, dma, g_o = _merge_bwd(dattn, w_o, mixed, ah, aa, p)
    else:
        gu_units = [halves(g_g), halves(g_u)]
        (dp, dmh, dma, g_o), gu_recv = _merge_bwd(dattn, w_o, mixed, ah, aa, p, carry=_carry_pairx(gu_units))
        ffn_pairs = list(dn_pairs) + list(_rs_pair_add(gu_units, gu_recv, core))
    dy_hg, dy_at, g_bh, g_ba = _branch_bwd(dmh, dma, bh4, ba4, y_hg, y_at)
    if dist is None:
        dp, do, dhw4 = _readout_bwd(o0, o1, p, hw4, dy_hg, dp)
        dq, dkw, dvw, dkc, dvc, dsk = _attn_bwd(qr, k4, v4, sinks, y_at, lse, dy_at)
        dp, dqnw8, dknw2 = _attn_post(p, cos, sin, qnw8, knw2, bd512, bd128, dupt, dq, dkw, dvw, dkc, dvc, dp)
    else:
        mix_units = [halves(g_bh), halves(g_ba), halves(g_o.reshape(4, D // 4, D))]
        (dp, do, dhw4), mix_recv = _readout_bwd(o0, o1, p, hw4, dy_hg, dp, carry=_carry_pairx(mix_units))
        mix_pairs = _rs_pair_add(mix_units, mix_recv, core)
        (dq, dkw, dvw, dkc, dvc, dsk), bwd = _attn_bwd(
            qr, k4, v4, sinks, y_at, lse, dy_at,
            carry=_carry_join(_carry_chipx(ffn_pairs[1:2]), _carry_sibx(red_dn)))
        red_g = _rs_chip_add(ffn_pairs[1:2], bwd[0:1], core, chip)
        (dp, dqnw8, dknw2), post = _attn_post(
            p, cos, sin, qnw8, knw2, bd512, bd128, dupt, dq, dkw, dvw, dkc, dvc, dp, carry=_carry_sibx(red_g))
    if dist is None:
        dp, dv0, dq0, dlg0 = _hgrn_bwd(p, lg, do, st0, dp, None, rev=False)
        dp, dlg1 = _hgrn_bwd(p, lg, do, st1, dp, (dv0, dq0), rev=True)
    else:
        (dp, dv0, dq0, dlg0), c_u = _hgrn_bwd(p, lg, do, st0, dp, None, rev=False,
                                              carry=_carry_chipx(ffn_pairs[2:3]))
        red_u = _rs_chip_add(ffn_pairs[2:3], c_u, core, chip)
        (dp, dlg1), last = _hgrn_bwd(p, lg, do, st1, dp, (dv0, dq0), rev=True,
                                     carry=_carry_join(_carry_chipx(mix_pairs), _carry_sibx(red_u)))
        mix_reds = _rs_chip_add(mix_pairs, last[0:3], core, chip)
        ffn_done = bwd[1:2] + post[0:1] + last[3:4]
    g_in = _mm_gin(dp, h, tmt)
    if dist is None:
        dh = _mm_dh(dp, w_in, tmt)
        gx, dss1, dnw1 = _mod1_bwd(ctx, x, dh, dx1, nw1, ss1)
        rs = None
    else:
        in_units = [halves(g_in.reshape(4, NCOL // 4, D))]
        dh, both = _mm_dh(dp, w_in, tmt, carry=_carry_join(_carry_pairx(in_units), _carry_sibx(mix_reds)))
        in_recv, mix_done = both[0:1], both[1:4]
        in_pairs = _rs_pair_add(in_units, in_recv, core)
        first_rows = (0, in_pairs[0].shape[1] // 2)
        (gx, dss1, dnw1), in_part = _mod1_bwd(ctx, x, dh, dx1, nw1, ss1,
                                              carry=_carry_chipx(in_pairs, rows=first_rows))
        rs = dict(ffn_done=ffn_done, mix_done=mix_done, in_pairs=in_pairs, in_part=in_part)

    dmod = jnp.concatenate([dss1[1], dg1, dss2, dg2], axis=0)
    dmodc = dss1[0]
    raw = (dss1, dg1, dss2, dg2, dnw1, dnw2, dhw4, dqnw8, dknw2, dsk, dlg0, dlg1)
    small = dict(raw=raw, dmod=dmod, dmodc=dmodc, dnw1=dnw1, dnw2=dnw2,
                 dhw=dhw4.reshape(4, HGD).sum(0, keepdims=True),
                 dqnw=dqnw8.reshape(8, HDIM).sum(0, keepdims=True),
                 dknw=dknw2.reshape(2, HDIM).sum(0, keepdims=True),
                 dsinks=dsk[:, 0], dlg=jnp.concatenate([dlg0, dlg1], axis=0))
    big = dict(w_in=g_in, w_bh=g_bh, w_ba=g_ba, w_o=g_o, w_g=g_g, w_u=g_u, w_dn=g_dn)
    return sq, gx, big, small, rs


def _place():
    x, y, c = lax.axis_index("x"), lax.axis_index("y"), lax.axis_index("c")
    return x, y, c


def _gather_blocks(x_refs, out_refs, send_sems, recv_sems, local_sems):
    n = len(out_refs)
    x, y, c = _place()
    me, sibling = (x, y, c), (x, y, 1 - c)
    chips = [(1 - x, y), (x, 1 - y), (1 - x, 1 - y)]

    def slot(u, px, py, pc):
        return out_refs[u].at[4 * px + 2 * py + pc]

    def copy(u, k, block, to, src=None):
        return pltpu.make_async_remote_copy(
            src_ref=slot(u, *block) if src is None else src, dst_ref=slot(u, *block),
            send_sem=send_sems.at[u, k], recv_sem=recv_sems.at[u, k], device_id=to, device_id_type=MESH)

    mines = []
    if x_refs is not None:
        mines = [pltpu.make_async_copy(x_refs[u], slot(u, *me), local_sems.at[u]) for u in range(n)]
    for cp in mines:
        cp.start()
    first = []
    for u in range(n):
        src = None if x_refs is None else x_refs[u]
        first.append(copy(u, 0, me, sibling, src=src))
        first += [copy(u, 1 + j, me, (*chip, c), src=src) for j, chip in enumerate(chips)]
    for cp in first:
        cp.start()
    passed = []
    for j, chip in enumerate(chips):
        for u in range(n):
            copy(u, 1 + j, (*chip, c), me).wait_recv()
            fwd = copy(u, 4 + j, (*chip, c), sibling)
            fwd.start()
            passed.append(fwd)
    for u in range(n):
        copy(u, 0, sibling, me).wait_recv()
    for j, chip in enumerate(chips):
        for u in range(n):
            copy(u, 4 + j, (*chip, 1 - c), me).wait_recv()
    for cp in first + passed:
        cp.wait_send()
    for cp in mines:
        cp.wait()


def _gather_sems(n):
    return [pltpu.SemaphoreType.DMA((n, 7)), pltpu.SemaphoreType.DMA((n, 7)), pltpu.SemaphoreType.DMA((n,))]


def _allgather(blks, *, name, in_vmem):
    n = len(blks)
    space = pltpu.VMEM if in_vmem else pl.ANY

    def body(*refs):
        _gather_blocks(refs[:n], refs[n:2 * n], *refs[2 * n:])

    return pl.pallas_call(
        body, name=name, out_shape=[jax.ShapeDtypeStruct((8,) + b.shape, b.dtype) for b in blks],
        in_specs=[pl.BlockSpec(memory_space=space)] * n, out_specs=[pl.BlockSpec(memory_space=space)] * n,
        scratch_shapes=_gather_sems(n))(*blks)


def _cast_place(ws, c, dev):
    n = len(ws)

    def body(s_ref, *refs):
        for u in range(n):
            refs[n + u][0] = refs[u][...].astype(BF16)

    in_specs, out_specs, out_shape = [], [], []
    for w in ws:
        q, cols = w.shape[0] // 4, w.shape[1]
        in_specs.append(pl.BlockSpec((q, cols), lambda i, s: (2 * s[0] + i, 0)))
        out_specs.append(pl.BlockSpec((1, q, cols), lambda i, s: (s[1], i, 0)))
        out_shape.append(jax.ShapeDtypeStruct((8, 2 * q, cols), BF16))
    return pl.pallas_call(
        body, name="cast_place",
        grid_spec=pltpu.PrefetchScalarGridSpec(num_scalar_prefetch=1, grid=(2,), in_specs=in_specs,
                                               out_specs=out_specs),
        out_shape=_out_hbm(out_shape),
        compiler_params=pltpu.CompilerParams(vmem_limit_bytes=48 << 20))(jnp.stack([c, dev]), *_in_hbm(ws))


def _gather_phases(out_refs, send_sems, recv_sems, rows=None):
    n = len(out_refs)
    x, y, c = _place()
    me, sibling = (x, y, c), (x, y, 1 - c)
    chips = [(1 - x, y), (x, 1 - y), (1 - x, 1 - y)]

    def copy(u, k, block, to):
        px, py, pc = block
        ref = out_refs[u].at[4 * px + 2 * py + pc]
        if rows is not None and rows[u] is not None:
            ref = ref.at[pl.ds(rows[u][0], rows[u][1])]
        return pltpu.make_async_remote_copy(src_ref=ref, dst_ref=ref, send_sem=send_sems.at[u, k],
                                            recv_sem=recv_sems.at[u, k], device_id=to, device_id_type=MESH)

    def start():
        for u in range(n):
            copy(u, 0, me, sibling).start()
            for j, chip in enumerate(chips):
                copy(u, 1 + j, me, (*chip, c)).start()

    def mid():
        for j, chip in enumerate(chips):
            for u in range(n):
                copy(u, 1 + j, (*chip, c), me).wait_recv()
                copy(u, 4 + j, (*chip, c), sibling).start()

    def end():
        for u in range(n):
            copy(u, 0, sibling, me).wait_recv()
        for j, chip in enumerate(chips):
            for u in range(n):
                copy(u, 4 + j, (*chip, 1 - c), me).wait_recv()
        for u in range(n):
            copy(u, 0, me, sibling).wait_send()
            for j, chip in enumerate(chips):
                copy(u, 1 + j, me, (*chip, c)).wait_send()
                copy(u, 4 + j, (*chip, c), sibling).wait_send()

    return start, mid, end


def _carry_gather(bufs, rows=None):
    n = len(bufs)
    return _Carry(bufs, [jax.ShapeDtypeStruct(b.shape, b.dtype) for b in bufs], {u: u for u in range(n)},
                  [pltpu.SemaphoreType.DMA((n, 7)), pltpu.SemaphoreType.DMA((n, 7))],
                  lambda ins, outs, sems: _gather_phases(outs, *sems, rows=rows))


def _allgather_inplace(bufs, *, name):
    n = len(bufs)

    def body(*refs):
        for phase in _gather_phases(refs[n:2 * n], *refs[2 * n:]):
            phase()

    return pl.pallas_call(
        body, name=name, out_shape=[jax.ShapeDtypeStruct(b.shape, b.dtype) for b in bufs],
        in_specs=[ANY] * n, out_specs=[ANY] * n, input_output_aliases={u: u for u in range(n)},
        scratch_shapes=[pltpu.SemaphoreType.DMA((n, 7)), pltpu.SemaphoreType.DMA((n, 7))])(*bufs)


def _ag_small(raw, sq):
    def body(dss1, dg1, dss2, dg2, dnw1, dnw2, dhw4, dqnw8, dknw2, dsk, dlg0, dlg1, sq_ref,
             out_ref, tot_ref, blk, send_sems, recv_sems, local_sems):
        blk[...] = jnp.zeros_like(blk)
        blk[0:2, :] = dss1[1]
        blk[2:3, :] = dg1[...]
        blk[3:5, :] = dss2[...]
        blk[5:6, :] = dg2[...]
        blk[6:8, :] = dss1[0]
        blk[8:9, :] = dnw1[...]
        blk[9:10, :] = dnw2[...]
        blk[10:11, 0:HGW] = dhw4[...]
        blk[10:11, HGW:D] = dqnw8[...]
        blk[11:12, 0:128] = dknw2[...]
        blk[12:14, 0:HGW] = dlg0[0]
        blk[14:16, 0:HGW] = dlg1[0]
        blk[16:24, 0:128] = dsk[...]
        blk[24:25, :] = sq_ref[...]
        _gather_blocks([blk], [out_ref], send_sems, recv_sems, local_sems)
        acc = out_ref[0]
        for i in range(1, 8):
            acc = acc + out_ref[i]
        tot_ref[...] = acc

    vm = pl.BlockSpec(memory_space=pltpu.VMEM)
    return pl.pallas_call(
        body, name="ag_small",
        out_shape=[jax.ShapeDtypeStruct((8, 32, D), F32), jax.ShapeDtypeStruct((32, D), F32)],
        in_specs=[vm] * 13, out_specs=[vm, vm],
        scratch_shapes=[pltpu.VMEM((32, D), F32)] + _gather_sems(1))(*raw, sq)


def _rs_pair_exchange(units):
    n = len(units)

    def body(*refs):
        start, _, end = _pairx_phases(refs[:n], refs[n:2 * n], *refs[2 * n:])
        start()
        end()

    return pl.pallas_call(
        body, name="rs_pair_exchange", out_shape=_pairx_shapes(units),
        in_specs=[ANY] * n, out_specs=[ANY] * n,
        scratch_shapes=[pltpu.SemaphoreType.DMA((n, 4)), pltpu.SemaphoreType.DMA((n, 4))])(*units)


def _pairx_shapes(units):
    return [jax.ShapeDtypeStruct((4,) + g.shape[2:], g.dtype) for g in units]


def _pairx_phases(g_refs, r_refs, send_sems, recv_sems):
    n = len(g_refs)
    x, y, c = _place()
    cps = [pltpu.make_async_remote_copy(
        src_ref=g_refs[u].at[j, 1 - c], dst_ref=r_refs[u].at[j], send_sem=send_sems.at[u, j],
        recv_sem=recv_sems.at[u, j], device_id=(x, y, 1 - c), device_id_type=MESH)
        for u in range(n) for j in range(4)]

    def start():
        for cp in cps:
            cp.start()

    def end():
        for cp in cps:
            cp.wait()

    return start, None, end


def _carry_pairx(units):
    n = len(units)
    return _Carry(units, _pairx_shapes(units), {},
                  [pltpu.SemaphoreType.DMA((n, 4)), pltpu.SemaphoreType.DMA((n, 4))],
                  lambda ins, outs, sems: _pairx_phases(ins, outs, *sems))


def _rs_pair_add(units, recvs, c):
    n = len(units)

    def body(c_ref, *refs):
        for u in range(n):
            refs[2 * n + u][...] = (refs[u][0].astype(F32) + refs[n + u][...].astype(F32)).astype(BF16)

    in_specs, out_specs, out_shape = [], [], []
    for g in units:
        h, w = g.shape[2] // 2, g.shape[3]
        in_specs.append(pl.BlockSpec((1, 1, h, w), lambda j, i, cr: (j, cr[0], i, 0)))
    for g in units:
        h, w = g.shape[2] // 2, g.shape[3]
        in_specs.append(pl.BlockSpec((1, h, w), lambda j, i, cr: (j, i, 0)))
        out_specs.append(pl.BlockSpec((1, h, w), lambda j, i, cr: (j, i, 0)))
        out_shape.append(jax.ShapeDtypeStruct((4, 2 * h, w), BF16))
    return pl.pallas_call(
        body, name="rs_pair_add",
        grid_spec=pltpu.PrefetchScalarGridSpec(num_scalar_prefetch=1, grid=(4, 2), in_specs=in_specs,
                                               out_specs=out_specs),
        out_shape=_out_hbm(out_shape),
        compiler_params=pltpu.CompilerParams(vmem_limit_bytes=48 << 20))(
            c.reshape(1), *_in_hbm(list(units) + list(recvs)))


def _rs_chip_exchange(pairs):
    n = len(pairs)

    def body(*refs):
        start, _, end = _chipx_phases(refs[:n], refs[n:2 * n], *refs[2 * n:])
        start()
        end()

    return pl.pallas_call(
        body, name="rs_chip_exchange", out_shape=[jax.ShapeDtypeStruct(p.shape, p.dtype) for p in pairs],
        in_specs=[ANY] * n, out_specs=[ANY] * n,
        scratch_shapes=[pltpu.SemaphoreType.DMA((n, 3)), pltpu.SemaphoreType.DMA((n, 3))])(*pairs)


def _chipx_phases(p_refs, r_refs, send_sems, recv_sems, rows=None):
    n = len(p_refs)
    x, y, c = _place()
    k = 2 * x + y

    def part(ref):
        return ref if rows is None else ref.at[pl.ds(rows[0], rows[1])]

    sends = []
    for d in range(1, 4):
        j = (k + d) % 4
        for u in range(n):
            sends.append(pltpu.make_async_remote_copy(
                src_ref=part(p_refs[u].at[j]), dst_ref=part(r_refs[u].at[k]), send_sem=send_sems.at[u, d - 1],
                recv_sem=recv_sems.at[u, d - 1], device_id=(j // 2, j % 2, c), device_id_type=MESH))

    def start():
        for cp in sends:
            cp.start()

    def end():
        for d in range(1, 4):
            src = (k + 4 - d) % 4
            for u in range(n):
                pltpu.make_async_remote_copy(
                    src_ref=part(p_refs[u].at[src]), dst_ref=part(r_refs[u].at[src]),
                    send_sem=send_sems.at[u, d - 1], recv_sem=recv_sems.at[u, d - 1], device_id=(x, y, c),
                    device_id_type=MESH).wait_recv()
        for cp in sends:
            cp.wait_send()

    return start, None, end


def _carry_chipx(pairs, rows=None, into=None):
    n = len(pairs)
    sems = [pltpu.SemaphoreType.DMA((n, 3)), pltpu.SemaphoreType.DMA((n, 3))]
    shapes = [jax.ShapeDtypeStruct(p.shape, p.dtype) for p in pairs]
    if into is None:
        return _Carry(pairs, shapes, {}, sems, lambda ins, outs, s: _chipx_phases(ins, outs, *s, rows=rows))
    return _Carry(list(pairs) + list(into), shapes, {n + u: u for u in range(n)}, sems,
                  lambda ins, outs, s: _chipx_phases(ins[:n], outs, *s, rows=rows))


def _rs_chip_add(pairs, contribs, c, chip):
    n = len(pairs)

    def body(s_ref, *refs):
        for u in range(n):
            a, b, c_, d = refs[4 * u:4 * u + 4]
            refs[4 * n + u][0] = ((a[0].astype(F32) + b[0].astype(F32)) + c_[0].astype(F32)) + d[0].astype(F32)

    in_specs, out_specs, out_shape, args = [], [], [], []
    for p, r in zip(pairs, contribs):
        h, w = p.shape[1] // 2, p.shape[2]
        in_specs += [pl.BlockSpec((1, h, w), functools.partial(lambda d, i, s: ((s[1] + d) % 4, i, 0), d))
                     for d in range(4)]
        args += [p, r, r, r]
        out_specs.append(pl.BlockSpec((1, h, w), lambda i, s: (s[0], i, 0)))
        out_shape.append(jax.ShapeDtypeStruct((2, 2 * h, w), F32))
    return pl.pallas_call(
        body, name="rs_chip_add",
        grid_spec=pltpu.PrefetchScalarGridSpec(num_scalar_prefetch=1, grid=(2,), in_specs=in_specs,
                                               out_specs=out_specs),
        out_shape=_out_hbm(out_shape),
        compiler_params=pltpu.CompilerParams(vmem_limit_bytes=48 << 20))(jnp.stack([c, chip]), *_in_hbm(args))


def _rs_sibling_gather(reds):
    n = len(reds)

    def body(*refs):
        start, _, end = _sibx_phases(refs[n:2 * n], *refs[2 * n:])
        start()
        end()

    return pl.pallas_call(
        body, name="rs_sibling_gather", out_shape=[jax.ShapeDtypeStruct(r.shape, r.dtype) for r in reds],
        in_specs=[ANY] * n, out_specs=[ANY] * n, input_output_aliases={u: u for u in range(n)},
        scratch_shapes=[pltpu.SemaphoreType.DMA((n,))] * 2)(*reds)


def _sibx_phases(o_refs, send_sems, recv_sems):
    n = len(o_refs)
    x, y, c = _place()
    cps = [pltpu.make_async_remote_copy(
        src_ref=o_refs[u].at[c], dst_ref=o_refs[u].at[c], send_sem=send_sems.at[u], recv_sem=recv_sems.at[u],
        device_id=(x, y, 1 - c), device_id_type=MESH) for u in range(n)]

    def start():
        for cp in cps:
            cp.start()

    def end():
        for u in range(n):
            cps[u].wait_send()
            pltpu.make_async_remote_copy(
                src_ref=o_refs[u].at[1 - c], dst_ref=o_refs[u].at[1 - c], send_sem=send_sems.at[u],
                recv_sem=recv_sems.at[u], device_id=(x, y, 1 - c), device_id_type=MESH).wait_recv()

    return start, None, end


def _carry_sibx(reds):
    n = len(reds)
    return _Carry(reds, [jax.ShapeDtypeStruct(r.shape, r.dtype) for r in reds], {u: u for u in range(n)},
                  [pltpu.SemaphoreType.DMA((n,))] * 2, lambda ins, outs, sems: _sibx_phases(outs, *sems))


def _prologue(blk, c_ctx, w, b, in8):
    n = w.shape[1]

    def body(blk_ref, cctx_ref, w_ref, b_ref, _in_in, g0_ref, c16_ref, g1_ref, in_ref, mod_s,
             s1, r1, l1, s2, r2, l2, s3, r3):
        start, mid, end = _gather_phases([in_ref], s3, r3)
        start()
        _gather_blocks([blk_ref], [g0_ref], s1, r1, l1)
        c16 = jnp.concatenate([g0_ref[i, 0:1, :] for i in range(8)] + [cctx_ref[...], jnp.zeros((7, D), F32)],
                              axis=0)
        c16_ref[...] = c16
        mod_s[...] = _dot(c16 * _sig(c16), w_ref[...], prec=HI) + b_ref[...]
        _gather_blocks([mod_s], [g1_ref], s2, r2, l2)
        mid()
        end()

    vm = pl.BlockSpec(memory_space=pltpu.VMEM)
    return pl.pallas_call(
        body, name="prologue",
        out_shape=[jax.ShapeDtypeStruct((8, 8, D), F32), jax.ShapeDtypeStruct((16, D), F32),
                   jax.ShapeDtypeStruct((8, 16, n), F32), jax.ShapeDtypeStruct(in8.shape, in8.dtype)],
        in_specs=[vm, vm, vm, vm, ANY], out_specs=[vm, vm, vm, ANY], input_output_aliases={4: 3},
        scratch_shapes=[pltpu.VMEM((16, n), F32)] + _gather_sems(1) + _gather_sems(1)
        + [pltpu.SemaphoreType.DMA((1, 7)), pltpu.SemaphoreType.DMA((1, 7))],
        compiler_params=pltpu.CompilerParams(vmem_limit_bytes=48 << 20))(blk, c_ctx, w, b, in8)


def _ada_bwd(c16, dmod16, w, carry=None):
    n = w.shape[1]
    tn = 512

    def body(c_ref, d_ref, w_ref, gw_ref, gc_ref):
        j = pl.program_id(0)

        @pl.when(j == 0)
        def _():
            gc_ref[...] = jnp.zeros_like(gc_ref)

        cc = c_ref[...]
        dm = d_ref[...]
        gw_ref[...] = _dot(cc * _sig(cc), dm, TN, prec=HI)
        gc_ref[...] += _dot(dm, w_ref[...], NT, prec=HI)

    return _pcall(body, name="ada_bwd", grid=(n // tn,),
                  in_specs=[_full((16, D)), pl.BlockSpec((16, tn), lambda j: (0, j)),
                            pl.BlockSpec((D, tn), lambda j: (0, j))],
                  out_specs=[pl.BlockSpec((D, tn), lambda j: (0, j)), _full((16, D))],
                  out_shape=[jax.ShapeDtypeStruct((D, n), F32),
                             jax.ShapeDtypeStruct((16, D), F32)], carry=carry)(c16, dmod16, w)


def _adam_math(w, g, m, v):
    c1 = 1.0 - ADAM_B1 ** ADAM_STEP
    c2 = 1.0 - ADAM_B2 ** ADAM_STEP
    nm = ADAM_B1 * m + (1.0 - ADAM_B1) * g
    nv = ADAM_B2 * v + (1.0 - ADAM_B2) * (g * g)
    return -ADAM_LR * ((nm / c1) / (jnp.sqrt(nv / c2) + ADAM_EPS) + ADAM_WD * w), nm, nv


def _adamw_small(ws, gs, ms, vs):
    n = len(ws)

    def body(*refs):
        for u in range(n):
            d_, nm, nv = _adam_math(refs[u][...], refs[n + u][...], refs[2 * n + u][...], refs[3 * n + u][...])
            refs[4 * n + u][...] = d_
            refs[5 * n + u][...] = nm
            refs[6 * n + u][...] = nv

    specs = [_full(w.shape) for w in ws]
    shapes = [jax.ShapeDtypeStruct(w.shape, F32) for w in ws]
    out = _pcall(body, name="adamw_small", grid=(1,), in_specs=specs * 4, out_specs=specs * 3,
                 out_shape=shapes * 3)(*ws, *gs, *ms, *vs)
    return out[:n], out[n:2 * n], out[2 * n:]


def _cctx_grad(parts, c_ctx):
    def body(p_ref, c_ref, o_ref):
        acc = p_ref[0:1, :]
        for k in range(1, 4):
            acc = acc + p_ref[k:k + 1, :]
        cc = c_ref[...]
        s = _sig(cc)
        o_ref[...] = acc * (s * (1.0 + cc * (1.0 - s)))

    return _pcall(body, name="cctx_grad", grid=(1,), in_specs=[_full(parts.shape), _full((1, D))],
                  out_specs=_full((1, D)), out_shape=jax.ShapeDtypeStruct((1, D), F32))(parts, c_ctx)


ADAM_STEPS = 8


def _adamw_multi(ws, gs, ms, vs, *, name, carry=None):
    n = len(ws)

    def body(*refs):
        for u in range(n):
            refs[4 * n + u][...], refs[5 * n + u][...], refs[6 * n + u][...] = _adam_math(
                refs[u][...], refs[n + u][...], refs[2 * n + u][...], refs[3 * n + u][...])

    specs = [pl.BlockSpec((w.shape[0] // ADAM_STEPS, w.shape[1]), lambda i: (i, 0)) for w in ws]
    shapes = [jax.ShapeDtypeStruct(w.shape, F32) for w in ws]
    res = _pcall(body, name=name, grid=(ADAM_STEPS,), in_specs=specs * 4, out_specs=specs * 3,
                 out_shape=shapes * 3, carry=carry)(*ws, *gs, *ms, *vs)
    out, extra = res if carry is not None else (res, None)
    return (out[:n], out[n:2 * n], out[2 * n:]), extra


def kernel(x, c, ctx, c_ctx, w_ada, b_ada, norm_mix_w, norm_ffn_w, w_in, hgrn_lb_logits, hgrn_norm_w, q_norm_w, k_norm_w, attn_sinks, w_branch_hgrn, w_branch_attn, w_out, w_ffn_gate, w_ffn_up, w_ffn_down, loss_target, m_c_ctx, m_w_ada, m_b_ada, m_norm_mix_w, m_norm_ffn_w, m_w_in, m_hgrn_lb_logits, m_hgrn_norm_w, m_q_norm_w, m_k_norm_w, m_attn_sinks, m_w_branch_hgrn, m_w_branch_attn, m_w_out, m_w_ffn_gate, m_w_ffn_up, m_w_ffn_down, v_c_ctx, v_w_ada, v_b_ada, v_norm_mix_w, v_norm_ffn_w, v_w_in, v_hgrn_lb_logits, v_hgrn_norm_w, v_q_norm_w, v_k_norm_w, v_attn_sinks, v_w_branch_hgrn, v_w_branch_attn, v_w_out, v_w_ffn_gate, v_w_ffn_up, v_w_ffn_down):
    xi, yi, ci = _place()
    chip = 2 * xi + yi
    dev = 2 * chip + ci
    s_len = x.shape[1]

    shards = [w_in[0].T, w_branch_hgrn[0], w_branch_attn[0], w_out[0], w_ffn_gate[0].T, w_ffn_up[0].T,
              w_ffn_down[0]]
    bufs = _cast_place(shards, ci, dev)

    lbrow = jnp.pad(hgrn_lb_logits.reshape(1, 512), ((0, 0), (0, D - 512)))
    blk = jnp.concatenate([c, lbrow, jnp.zeros((6, D), F32)], axis=0)
    nada = w_ada.shape[2]
    b_sh = lax.dynamic_slice(b_ada, (0, chip * nada), (1, nada))
    g0, c16, g1, in8 = _prologue(blk, c_ctx[None], w_ada[0], b_sh, bufs[0])
    lg = g0[0::2, 1, :512].reshape(4, 2, 2, 128).transpose(1, 2, 0, 3).reshape(2, 2, HGW)
    modall = g1[0::2].transpose(1, 0, 2).reshape(16, 4 * nada)
    mod = lax.dynamic_slice(modall, (dev, 0), (1, 6 * D)).reshape(6, D)
    modc = modall[8].reshape(6, D)[:2]

    sq, gx, _, small, rs = _local_step(
        x[0], ctx[0], loss_target[0], mod, modc, norm_mix_w, norm_ffn_w, lg, hgrn_norm_w, q_norm_w,
        k_norm_w, attn_sinks[0], in8.reshape(NCOL, D), bufs[1:], dist=(ci, chip))

    def whole(r):
        return r.reshape(2 * r.shape[1], r.shape[2])

    g_dn, g_g, g_u = [whole(r) for r in rs["ffn_done"]]
    g_bh, g_ba, g_o = [whole(r) for r in rs["mix_done"]]
    in_pairs = rs["in_pairs"]
    rest_rows = (in_pairs[0].shape[1] // 2, in_pairs[0].shape[1] // 2)

    g2, tot = _ag_small(small["raw"], sq)
    loss = 0.5 * jnp.sum(tot[24]) / D
    dmodc_tot = jnp.pad(tot[6:8].reshape(1, 2 * D), ((0, 0), (0, 4 * D)))
    g_b_ada = tot[0:6].reshape(1, 6 * D) + dmodc_tot
    dmod16 = jnp.concatenate([g2[:, 0:6].reshape(8, 6 * D), dmodc_tot, jnp.zeros((7, 6 * D), F32)], axis=0)
    (g_w_ada, gc_part), in_contribs = _ada_bwd(
        c16, lax.dynamic_slice(dmod16, (0, chip * nada), (16, nada)), w_ada[0],
        carry=_carry_chipx(in_pairs, rows=rest_rows, into=rs["in_part"]))
    g3, = _allgather([gc_part[8:16]], name="ag_cctx", in_vmem=True)
    g_c_ctx = _cctx_grad(g3[0::2, 0], c_ctx[None])[0]
    g_nw1 = tot[8:9]
    g_nw2 = tot[9:10]
    g_hw = tot[10, :HGW].reshape(4, HGD).sum(0, keepdims=True)
    g_qnw = tot[10, HGW:].reshape(8, HDIM).sum(0, keepdims=True)
    g_knw = tot[11, :128].reshape(2, HDIM).sum(0, keepdims=True)
    g_sinks = tot[16:24, 0][None]
    g_lg = lax.dynamic_slice(tot[12:16, :HGW].reshape(2, 2, HGW), (0, 0, chip * 128), (2, 2, 128))

    names = ["c_ctx", "w_ada", "b_ada", "norm_mix_w", "norm_ffn_w", "w_in", "hgrn_lb_logits", "hgrn_norm_w",
             "q_norm_w", "k_norm_w", "attn_sinks", "w_branch_hgrn", "w_branch_attn", "w_out", "w_ffn_gate",
             "w_ffn_up", "w_ffn_down"]
    ws = dict(zip(names, [c_ctx, w_ada, b_ada, norm_mix_w, norm_ffn_w, w_in, hgrn_lb_logits, hgrn_norm_w,
                          q_norm_w, k_norm_w, attn_sinks, w_branch_hgrn, w_branch_attn, w_out, w_ffn_gate,
                          w_ffn_up, w_ffn_down]))
    ms = dict(zip(names, [m_c_ctx, m_w_ada, m_b_ada, m_norm_mix_w, m_norm_ffn_w, m_w_in, m_hgrn_lb_logits,
                          m_hgrn_norm_w, m_q_norm_w, m_k_norm_w, m_attn_sinks, m_w_branch_hgrn,
                          m_w_branch_attn, m_w_out, m_w_ffn_gate, m_w_ffn_up, m_w_ffn_down]))
    vs = dict(zip(names, [v_c_ctx, v_w_ada, v_b_ada, v_norm_mix_w, v_norm_ffn_w, v_w_in, v_hgrn_lb_logits,
                          v_hgrn_norm_w, v_q_norm_w, v_k_norm_w, v_attn_sinks, v_w_branch_hgrn,
                          v_w_branch_attn, v_w_out, v_w_ffn_gate, v_w_ffn_up, v_w_ffn_down]))
    transposed = ("w_in", "w_ffn_gate", "w_ffn_up")

    def view(a, n):
        return a[0].T if n in transposed else a[0]

    def unview(a, n):
        return a.T[None] if n in transposed else a[None]

    delta, new_m, new_v, grads = {}, {}, {}, {}

    def big_adamw(group, gs, name, carry=None):
        (d_, m_, v_), extra = _adamw_multi([view(ws[n], n) for n in group], gs, [view(ms[n], n) for n in group],
                                           [view(vs[n], n) for n in group], name=name, carry=carry)
        for i, n in enumerate(group):
            grads[n], delta[n], new_m[n], new_v[n] = (unview(gs[i], n), unview(d_[i], n), unview(m_[i], n),
                                                      unview(v_[i], n))
        return extra

    big_adamw(["w_ffn_down", "w_ffn_gate", "w_ffn_up", "w_out", "w_branch_hgrn", "w_branch_attn"],
              [g_dn, g_g, g_u, g_o, g_bh, g_ba], "adamw_first")
    in_reds = _rs_chip_add(in_pairs, in_contribs, ci, chip)
    g_in, = [whole(r) for r in _rs_sibling_gather(in_reds)]
    big_adamw(["w_in", "w_ada"], [g_in, g_w_ada], "adamw_second")
    grads.update(c_ctx=g_c_ctx, b_ada=g_b_ada, norm_mix_w=g_nw1, norm_ffn_w=g_nw2, hgrn_lb_logits=g_lg,
                 hgrn_norm_w=g_hw, q_norm_w=g_qnw, k_norm_w=g_knw, attn_sinks=g_sinks)
    small_names = [n for n in names if n not in delta]

    def two_d(a):
        return a.reshape(1, -1) if a.ndim == 1 else a

    sd, sm_, sv = _adamw_small(*[[two_d(d[n]) for n in small_names] for d in (ws, grads, ms, vs)])
    for i, n in enumerate(small_names):
        for dst, src in ((delta, sd), (new_m, sm_), (new_v, sv)):
            dst[n] = src[i].reshape(ws[n].shape)
    return (loss, gx[None], *[grads[n] for n in names], *[delta[n] for n in names],
            *[new_m[n] for n in names], *[new_v[n] for n in names])
```

```python
import functools

import numpy as np
import jax
import jax.numpy as jnp
from jax import lax
from jax.experimental import pallas as pl
from jax.experimental.pallas import tpu as pltpu

F32 = jnp.float32
BF16 = jnp.bfloat16
HI = lax.Precision.HIGHEST
MESH = pl.DeviceIdType.MESH

D = 1024
L = 256
TM = 256
HGW = 512
HGD = 128
CH = 32
ATW = 512
HDIM = 64
BLK = 128
GRID_W = 64
DFF = 2816
NCOL = 5376
EPS = 1e-6
ROPE_THETA = 10000.0

C_FB, C_INP, C_QHG, C_FF = 0, 1, 2, 3
C_GATES = 1
C_GHG, C_QRAW = 8, 9
C_KV = 20
C_QKV = 6

ADAM_LR, ADAM_B1, ADAM_B2, ADAM_EPS, ADAM_WD, ADAM_STEP = 0.001, 0.9, 0.999, 1e-08, 0.01, 10

NN = (((1,), (0,)), ((), ()))
NT = (((1,), (1,)), ((), ()))
TN = (((0,), (0,)), ((), ()))


def _dot(a, b, dims=NN, prec=None):
    return lax.dot_general(a, b, dims, precision=prec, preferred_element_type=F32)


def _bdot(a, b, dims=NN):
    return _dot(a.astype(BF16), b.astype(BF16), dims)


def _sig(x):
    return 1.0 / (1.0 + jnp.exp(-x))


class _Carry:
    def __init__(self, ins, outs, aliases, scratch, phases):
        self.ins, self.outs, self.aliases, self.scratch, self.phases = ins, outs, aliases, scratch, phases


def _in_hbm(args):
    return [pltpu.with_memory_space_constraint(a, pltpu.HBM) for a in args]


def _out_hbm(shapes):
    if isinstance(shapes, (list, tuple)):
        return [pltpu.HBM(s.shape, s.dtype) for s in shapes]
    return pltpu.HBM(shapes.shape, shapes.dtype)


def _carry_join(a, b):
    na_in, na_out, na_sc = len(a.ins), len(a.outs), len(a.scratch)
    aliases = dict(a.aliases)
    aliases.update({na_in + i: na_out + o for i, o in b.aliases.items()})

    def phases(ins, outs, sems):
        pa = a.phases(ins[:na_in], outs[:na_out], sems[:na_sc])
        pb = b.phases(ins[na_in:], outs[na_out:], sems[na_sc:])

        def both(fa, fb):
            if fa is None and fb is None:
                return None

            def run():
                for fn in (fa, fb):
                    if fn is not None:
                        fn()
            return run

        return tuple(both(fa, fb) for fa, fb in zip(pa, pb))

    return _Carry(list(a.ins) + list(b.ins), list(a.outs) + list(b.outs), aliases,
                  list(a.scratch) + list(b.scratch), phases)


def _pcall(body, *, name, grid, in_specs, out_specs, out_shape, scratch=(), aliases=None, vmem_mb=48,
           carry=None):
    params = pltpu.CompilerParams(dimension_semantics=("arbitrary",) * len(grid),
                                  vmem_limit_bytes=vmem_mb << 20)
    if carry is None:
        plain = pl.pallas_call(
            body, name=name, grid=grid, in_specs=in_specs, out_specs=out_specs, out_shape=_out_hbm(out_shape),
            scratch_shapes=list(scratch), input_output_aliases=aliases or {}, compiler_params=params)
        return lambda *args: plain(*_in_hbm(args))
    single = not isinstance(out_shape, (list, tuple))
    out_specs_l = [out_specs] if single else list(out_specs)
    out_shape_l = [out_shape] if single else list(out_shape)
    n_in, n_out, n_sc = len(in_specs), len(out_shape_l), len(scratch)
    k_in, k_out = len(carry.ins), len(carry.outs)
    nsteps = int(np.prod(grid))
    assert nsteps >= 3

    def wrapped(*refs):
        ins, cins = refs[:n_in], refs[n_in:n_in + k_in]
        o0 = n_in + k_in
        outs, couts = refs[o0:o0 + n_out], refs[o0 + n_out:o0 + n_out + k_out]
        s0 = o0 + n_out + k_out
        sc, csc = refs[s0:s0 + n_sc], refs[s0 + n_sc:]
        step = pl.program_id(0)
        for ax in range(1, len(grid)):
            step = step * grid[ax] + pl.program_id(ax)
        start, mid, end = carry.phases(cins, couts, csc)
        pl.when(step == 0)(start)
        body(*ins, *outs, *sc)
        if mid is not None:
            pl.when(step == nsteps - 2)(mid)
        pl.when(step == nsteps - 1)(end)

    all_aliases = dict(aliases or {})
    all_aliases.update({n_in + i: n_out + o for i, o in carry.aliases.items()})
    call = pl.pallas_call(
        wrapped, name=name, grid=grid, in_specs=list(in_specs) + [ANY] * k_in,
        out_specs=out_specs_l + [ANY] * k_out, out_shape=_out_hbm(out_shape_l + list(carry.outs)),
        scratch_shapes=list(scratch) + list(carry.scratch), input_output_aliases=all_aliases,
        compiler_params=params)

    def run(*args):
        res = call(*_in_hbm(args), *carry.ins)
        core = res[:n_out]
        return (core[0] if single else list(core)), list(res[n_out:])

    return run


def _full(shape):
    nd = len(shape)
    return pl.BlockSpec(shape, lambda *_: (0,) * nd)


ANY = pl.BlockSpec(memory_space=pl.ANY)


def _mm(a, b, *, name, mode="nn", out_dtype=F32, tm, tn, tk):
    if mode == "nn":
        (m, k), (k2, n) = a.shape, b.shape
    elif mode == "nt":
        (m, k), (n, k2) = a.shape, b.shape
    else:
        (k, m), (k2, n) = a.shape, b.shape
    assert k == k2 and m % tm == 0 and n % tn == 0 and k % tk == 0, (name, a.shape, b.shape)
    nk = k // tk
    dims = {"nn": NN, "nt": NT, "tn": TN}[mode]

    def body(a_ref, b_ref, o_ref, acc):
        kk = pl.program_id(2)

        @pl.when(kk == 0)
        def _():
            acc[...] = jnp.zeros_like(acc)

        acc[...] += _bdot(a_ref[...], b_ref[...], dims)

        @pl.when(kk == nk - 1)
        def _():
            o_ref[...] = acc[...].astype(out_dtype)

    a_spec = (pl.BlockSpec((tk, tm), lambda i, j, kk: (kk, i)) if mode == "tn"
              else pl.BlockSpec((tm, tk), lambda i, j, kk: (i, kk)))
    b_spec = (pl.BlockSpec((tn, tk), lambda i, j, kk: (j, kk)) if mode == "nt"
              else pl.BlockSpec((tk, tn), lambda i, j, kk: (kk, j)))
    return _pcall(body, name=name, grid=(m // tm, n // tn, nk), in_specs=[a_spec, b_spec],
                  out_specs=pl.BlockSpec((tm, tn), lambda i, j, kk: (i, j)),
                  out_shape=jax.ShapeDtypeStruct((m, n), out_dtype),
                  scratch=[pltpu.VMEM((tm, tn), F32)])(a, b)


NT_IN = NCOL // 256


def _src_block(j):
    return j + jnp.where(j < 4, 2, jnp.where(j < 6, 3, jnp.where(j < 8, -6, jnp.where(
        j < 16, 5, jnp.where(j < 20, -7, -14)))))


def _mm_in(h, wt, tm, carry=None):
    tt = h.shape[0]

    def body(h_ref, w_ref, o_ref):
        o_ref[...] = _bdot(h_ref[...], w_ref[...], NT)

    return _pcall(body, name="mm_in", grid=(tt // tm, NT_IN),
                  in_specs=[pl.BlockSpec((tm, D), lambda i, j: (i, 0)),
                            pl.BlockSpec((256, D), lambda i, j: (_src_block(j), 0))],
                  out_specs=pl.BlockSpec((tm, 256), lambda i, j: (i, j)),
                  out_shape=jax.ShapeDtypeStruct((tt, NCOL), F32), carry=carry)(h, wt)


def _mm_dh(dp, wt, tm, carry=None):
    tt = dp.shape[0]
    per, ng = 3, NT_IN // 3

    def body(d_ref, w0, w1, w2, o_ref, acc):
        kk = pl.program_id(1)

        @pl.when(kk == 0)
        def _():
            acc[...] = jnp.zeros_like(acc)

        acc[...] += (_bdot(d_ref[:, 0:256], w0[...]) + _bdot(d_ref[:, 256:512], w1[...])
                     + _bdot(d_ref[:, 512:768], w2[...]))

        @pl.when(kk == ng - 1)
        def _():
            o_ref[...] = acc[...]

    wspecs = [pl.BlockSpec((256, D), functools.partial(lambda t, i, kk: (_src_block(per * kk + t), 0), t))
              for t in range(per)]
    return _pcall(body, name="mm_dh", grid=(tt // tm, ng),
                  in_specs=[pl.BlockSpec((tm, per * 256), lambda i, kk: (i, kk))] + wspecs,
                  out_specs=pl.BlockSpec((tm, D), lambda i, kk: (i, 0)),
                  out_shape=jax.ShapeDtypeStruct((tt, D), F32), scratch=[pltpu.VMEM((tm, D), F32)],
                  carry=carry)(dp, wt, wt, wt)


def _mm_gin(dp, h, tk):
    tt = dp.shape[0]
    nk = tt // tk

    def body(d_ref, h_ref, o_ref, acc):
        kk = pl.program_id(1)

        @pl.when(kk == 0)
        def _():
            acc[...] = jnp.zeros_like(acc)

        acc[...] += _bdot(d_ref[...], h_ref[...], TN)

        @pl.when(kk == nk - 1)
        def _():
            o_ref[...] = acc[...].astype(BF16)

    return _pcall(body, name="mm_gin", grid=(NT_IN, nk),
                  in_specs=[pl.BlockSpec((tk, 256), lambda j, kk: (kk, j)),
                            pl.BlockSpec((tk, D), lambda j, kk: (kk, 0))],
                  out_specs=pl.BlockSpec((256, D), lambda j, kk: (_src_block(j), 0)),
                  out_shape=jax.ShapeDtypeStruct((NCOL, D), BF16), scratch=[pltpu.VMEM((256, D), F32)])(dp, h)


def _tok_specs():
    assert L == TM
    return [_full((TM, D)), pl.BlockSpec((TM, D), lambda i: (jnp.maximum(i - 1, 0), 0))]


def _mod1(ctx, x, nw, ss):
    rows = L + x.shape[0]

    def body(c_ref, x_ref, nw_ref, ss_ref, h_ref):
        t = jnp.where(pl.program_id(0) == 0, c_ref[...], x_ref[...])
        r = lax.rsqrt(jnp.mean(t * t, axis=-1, keepdims=True) + EPS)
        s = ss_ref[0]
        h_ref[...] = ((t * r * nw_ref[...]) * (1.0 + s[1:2]) + s[0:1]).astype(BF16)

    return _pcall(body, name="mod1", grid=(rows // TM,),
                  in_specs=_tok_specs() + [_full((1, D)),
                                           pl.BlockSpec((1, 2, D), lambda i: (jnp.minimum(i, 1), 0, 0))],
                  out_specs=pl.BlockSpec((TM, D), lambda i: (i, 0)),
                  out_shape=jax.ShapeDtypeStruct((rows, D), BF16))(ctx, x, nw, ss)


def _norm_bwd_rows(x, dh, nw, scale):
    r = lax.rsqrt(jnp.mean(x * x, axis=-1, keepdims=True) + EPS)
    xh = x * r
    dxh = dh * ((1.0 + scale) * nw)
    dx = r * (dxh - xh * jnp.mean(dxh * xh, axis=-1, keepdims=True))
    return dx, xh


def _out_proj_mod2(mixed, w_o, x, g1, nw2, ss2):
    s_len = x.shape[0]
    tm = 512

    def body(m_ref, w_ref, x_ref, g_ref, nw_ref, ss_ref, ao_ref, x1_ref, h_ref):
        ao = _bdot(m_ref[...], w_ref[...])
        ao_ref[...] = ao.astype(BF16)
        x1 = x_ref[...] + g_ref[...] * ao
        x1_ref[...] = x1
        r = lax.rsqrt(jnp.mean(x1 * x1, axis=-1, keepdims=True) + EPS)
        s = ss_ref[0]
        h_ref[...] = ((x1 * r * nw_ref[...]) * (1.0 + s[1:2]) + s[0:1]).astype(BF16)

    row = pl.BlockSpec((tm, D), lambda i: (i, 0))
    f = jax.ShapeDtypeStruct((s_len, D), F32)
    return _pcall(body, name="out_proj_mod2", grid=(s_len // tm,),
                  in_specs=[row, _full((D, D)), row, _full((1, D)), _full((1, D)), _full((1, 2, D))],
                  out_specs=[row, row, row],
                  out_shape=[jax.ShapeDtypeStruct((s_len, D), BF16), f,
                             jax.ShapeDtypeStruct((s_len, D), BF16)])(mixed, w_o, x, g1, nw2, ss2)


TS = 1024


def _acc_call(body, *, name, grid, in_specs, out_specs, out_shape, acc_shapes, args, carry=None):
    return _pcall(body, name=name, grid=grid, in_specs=in_specs, out_specs=out_specs, out_shape=out_shape,
                  scratch=[pltpu.VMEM(s, F32) for s in acc_shapes], carry=carry)(*args)


def _mm_cs(a, w4, *, name):
    m, k = a.shape
    _, _, ns = w4.shape

    def body(a_ref, w_ref, o_ref):
        o_ref[...] = _bdot(a_ref[...], w_ref[0])

    return _pcall(body, name=name, grid=(m // TS, 4),
                  in_specs=[pl.BlockSpec((TS, k), lambda i, j: (i, 0)),
                            pl.BlockSpec((1, k, ns), lambda i, j: (j, 0, 0))],
                  out_specs=pl.BlockSpec((TS, ns), lambda i, j: (i, j)),
                  out_shape=jax.ShapeDtypeStruct((m, 4 * ns), F32))(a, w4)


def _mm_cs_nt(a, w4, *, name):
    m = a.shape[0]
    _, k, ns = w4.shape

    def body(a_ref, w_ref, o_ref, acc):
        j = pl.program_id(1)

        @pl.when(j == 0)
        def _():
            acc[...] = jnp.zeros_like(acc)

        acc[...] += _bdot(a_ref[...], w_ref[0], NT)

        @pl.when(j == 3)
        def _():
            o_ref[...] = acc[...]

    return _acc_call(body, name=name, grid=(m // TS, 4),
                     in_specs=[pl.BlockSpec((TS, ns), lambda i, j: (i, j)),
                               pl.BlockSpec((1, k, ns), lambda i, j: (j, 0, 0))],
                     out_specs=pl.BlockSpec((TS, k), lambda i, j: (i, 0)),
                     out_shape=jax.ShapeDtypeStruct((m, k), F32), acc_shapes=[(TS, k)], args=(a, w4))


def _mm_cs_tn(a, b, ns, *, name):
    s_len, k = a.shape
    nk = s_len // TS

    def body(a_ref, b_ref, o_ref, acc):
        t = pl.program_id(1)

        @pl.when(t == 0)
        def _():
            acc[...] = jnp.zeros_like(acc)

        acc[...] += _bdot(a_ref[...], b_ref[...], TN)

        @pl.when(t == nk - 1)
        def _():
            o_ref[0] = acc[...].astype(o_ref.dtype)

    return _acc_call(body, name=name, grid=(4, nk),
                     in_specs=[pl.BlockSpec((TS, k), lambda j, t: (t, 0)),
                               pl.BlockSpec((TS, ns), lambda j, t: (t, j))],
                     out_specs=pl.BlockSpec((1, k, ns), lambda j, t: (j, 0, 0)),
                     out_shape=jax.ShapeDtypeStruct((4, k, ns), BF16), acc_shapes=[(k, ns)], args=(a, b))


def _ffn_up(h2, g4, u4, carry=None):
    s_len = h2.shape[0]
    ns = g4.shape[1]

    def body(h_ref, g_ref, u_ref, a_ref, b_ref, z_ref):
        h = h_ref[...]
        a = _bdot(h, g_ref[0], NT)
        b = _bdot(h, u_ref[0], NT)
        a_ref[0] = a.astype(BF16)
        b_ref[0] = b.astype(BF16)
        z_ref[0] = (a * _sig(a) * b).astype(BF16)

    w = pl.BlockSpec((1, ns, D), lambda i, j: (j, 0, 0))
    o = pl.BlockSpec((1, TS, ns), lambda i, j: (j, i, 0))
    f = jax.ShapeDtypeStruct((4, s_len, ns), BF16)
    return _pcall(body, name="ffn_up", grid=(s_len // TS, 4),
                  in_specs=[pl.BlockSpec((TS, D), lambda i, j: (i, 0)), w, w], out_specs=[o, o, o],
                  out_shape=[f, f, jax.ShapeDtypeStruct((4, s_len, ns), BF16)], carry=carry)(h2, g4, u4)


def _ffn_down_loss(z4, dn4, x1, g2, tgt):
    _, s_len, ns = z4.shape

    def body(z_ref, w_ref, x1_ref, g_ref, t_ref, sq_ref, dx2_ref, dyb_ref, dg_ref, acc):
        i, j = pl.program_id(0), pl.program_id(1)

        @pl.when((i == 0) & (j == 0))
        def _():
            sq_ref[...] = jnp.zeros_like(sq_ref)
            dg_ref[...] = jnp.zeros_like(dg_ref)

        @pl.when(j == 0)
        def _():
            acc[...] = jnp.zeros_like(acc)

        acc[...] += _bdot(z_ref[0], w_ref[0])

        @pl.when(j == 3)
        def _():
            y_ = acc[...]
            g = g_ref[...]
            e = x1_ref[...] + g * y_ - t_ref[...]
            sq_ref[...] += jnp.sum(e * e, axis=0, keepdims=True)
            dx2 = e * (1.0 / D)
            dx2_ref[...] = dx2
            dyb_ref[...] = (g * dx2).astype(BF16)
            dg_ref[...] += jnp.sum(dx2 * y_, axis=0, keepdims=True)

    row = pl.BlockSpec((TS, D), lambda i, j: (i, 0))
    vec = _full((1, D))
    return _acc_call(body, name="ffn_down_loss", grid=(s_len // TS, 4),
                     in_specs=[pl.BlockSpec((1, TS, ns), lambda i, j: (j, i, 0)),
                               pl.BlockSpec((1, ns, D), lambda i, j: (j, 0, 0)), row, vec, row],
                     out_specs=[vec, row, row, vec],
                     out_shape=[jax.ShapeDtypeStruct((1, D), F32), jax.ShapeDtypeStruct((s_len, D), F32),
                                jax.ShapeDtypeStruct((s_len, D), BF16), jax.ShapeDtypeStruct((1, D), F32)],
                     acc_shapes=[(TS, D)], args=(z4, dn4, x1, g2, tgt))


def _ffn_dz(dyb, dn4, a4, b4):
    _, s_len, ns = a4.shape

    def body(dy_ref, w_ref, a_ref, b_ref, da_ref, db_ref):
        dz = _bdot(dy_ref[...], w_ref[0], NT)
        a = a_ref[0].astype(F32)
        s = _sig(a)
        da_ref[0] = (dz * b_ref[0].astype(F32) * (s * (1.0 + a * (1.0 - s)))).astype(BF16)
        db_ref[0] = (dz * (a * s)).astype(BF16)

    t = pl.BlockSpec((1, TS, ns), lambda i, j: (j, i, 0))
    o = jax.ShapeDtypeStruct((4, s_len, ns), BF16)
    return _pcall(body, name="ffn_dz", grid=(s_len // TS, 4),
                  in_specs=[pl.BlockSpec((TS, D), lambda i, j: (i, 0)),
                            pl.BlockSpec((1, ns, D), lambda i, j: (j, 0, 0)), t, t],
                  out_specs=[t, t], out_shape=[o, o])(dyb, dn4, a4, b4)


def _ffn_gdn(z4, dyb):
    _, s_len, ns = z4.shape
    nk = s_len // TS

    def body(z_ref, dy_ref, o_ref, acc):
        t = pl.program_id(1)

        @pl.when(t == 0)
        def _():
            acc[...] = jnp.zeros_like(acc)

        acc[...] += _bdot(z_ref[0], dy_ref[...], TN)

        @pl.when(t == nk - 1)
        def _():
            o_ref[0] = acc[...].astype(o_ref.dtype)

    return _acc_call(body, name="ffn_gdn", grid=(4, nk),
                     in_specs=[pl.BlockSpec((1, TS, ns), lambda j, t: (j, t, 0)),
                               pl.BlockSpec((TS, D), lambda j, t: (t, 0))],
                     out_specs=pl.BlockSpec((1, ns, D), lambda j, t: (j, 0, 0)),
                     out_shape=jax.ShapeDtypeStruct((4, ns, D), BF16), acc_shapes=[(ns, D)], args=(z4, dyb))


def _ffn_dh2(da4, db4, g4, u4, carry=None):
    _, s_len, ns = da4.shape

    def body(da_ref, db_ref, g_ref, u_ref, o_ref, acc):
        j = pl.program_id(1)

        @pl.when(j == 0)
        def _():
            acc[...] = jnp.zeros_like(acc)

        acc[...] += _bdot(da_ref[0], g_ref[0]) + _bdot(db_ref[0], u_ref[0])

        @pl.when(j == 3)
        def _():
            o_ref[...] = acc[...]

    t = pl.BlockSpec((1, TS, ns), lambda i, j: (j, i, 0))
    w = pl.BlockSpec((1, ns, D), lambda i, j: (j, 0, 0))
    return _acc_call(body, name="ffn_dh2", grid=(s_len // TS, 4), in_specs=[t, t, w, w],
                     out_specs=pl.BlockSpec((TS, D), lambda i, j: (i, 0)),
                     out_shape=jax.ShapeDtypeStruct((s_len, D), F32), acc_shapes=[(TS, D)],
                     args=(da4, db4, g4, u4), carry=carry)


def _ffn_ggu(h2, da4, db4, carry=None):
    _, s_len, ns = da4.shape
    nk = s_len // TS

    def body(h_ref, da_ref, db_ref, gg_ref, gu_ref, acc_g, acc_u):
        t = pl.program_id(1)

        @pl.when(t == 0)
        def _():
            acc_g[...] = jnp.zeros_like(acc_g)
            acc_u[...] = jnp.zeros_like(acc_u)

        h = h_ref[...]
        acc_g[...] += _bdot(da_ref[0], h, TN)
        acc_u[...] += _bdot(db_ref[0], h, TN)

        @pl.when(t == nk - 1)
        def _():
            gg_ref[0] = acc_g[...].astype(BF16)
            gu_ref[0] = acc_u[...].astype(BF16)

    d = pl.BlockSpec((1, TS, ns), lambda j, t: (j, t, 0))
    o = pl.BlockSpec((1, ns, D), lambda j, t: (j, 0, 0))
    f = jax.ShapeDtypeStruct((4, ns, D), BF16)
    return _acc_call(body, name="ffn_ggu", grid=(4, nk),
                     in_specs=[pl.BlockSpec((TS, D), lambda j, t: (t, 0)), d, d], out_specs=[o, o],
                     out_shape=[f, f], acc_shapes=[(ns, D), (ns, D)], args=(h2, da4, db4), carry=carry)


def _mod2_bwd(x1, dh2, dx2, ao, nw2, ss2, g1):
    s_len = x1.shape[0]

    def body(x1_ref, dh_ref, dx2_ref, ao_ref, nw_ref, ss_ref, g_ref,
             dx1_ref, da_ref, dss_ref, dnw_ref, dg_ref):
        i = pl.program_id(0)

        @pl.when(i == 0)
        def _():
            dss_ref[...] = jnp.zeros_like(dss_ref)
            dnw_ref[...] = jnp.zeros_like(dnw_ref)
            dg_ref[...] = jnp.zeros_like(dg_ref)

        dh = dh_ref[...]
        nw = nw_ref[...]
        scale = ss_ref[0][1:2]
        dxn, xh = _norm_bwd_rows(x1_ref[...], dh, nw, scale)
        dx1 = dx2_ref[...] + dxn
        dx1_ref[...] = dx1
        da_ref[...] = (g_ref[...] * dx1).astype(BF16)
        dg_ref[...] += jnp.sum(dx1 * ao_ref[...].astype(F32), axis=0, keepdims=True)
        dsh = jnp.sum(dh, axis=0, keepdims=True)
        dsc = jnp.sum(dh * xh * nw, axis=0, keepdims=True)
        dss_ref[...] += jnp.concatenate([dsh, dsc], axis=0)
        dnw_ref[...] += jnp.sum(dh * xh * (1.0 + scale), axis=0, keepdims=True)

    row = pl.BlockSpec((TM, D), lambda i: (i, 0))
    vec = _full((1, D))
    return _pcall(body, name="mod2_bwd", grid=(s_len // TM,),
                  in_specs=[row, row, row, row, vec, _full((1, 2, D)), vec],
                  out_specs=[row, row, _full((2, D)), vec, vec],
                  out_shape=[jax.ShapeDtypeStruct((s_len, D), F32), jax.ShapeDtypeStruct((s_len, D), BF16),
                             jax.ShapeDtypeStruct((2, D), F32), jax.ShapeDtypeStruct((1, D), F32),
                             jax.ShapeDtypeStruct((1, D), F32)])(x1, dh2, dx2, ao, nw2, ss2, g1)


def _mod1_bwd(ctx, x, dh, dx1, nw1, ss1, carry=None):
    s_len = dx1.shape[0]
    tt = L + s_len

    def body(c_ref, x_ref, dh_ref, dx1_ref, nw_ref, ss_ref, dx_ref, dss_ref, dnw_ref):
        i = pl.program_id(0)
        tok = jnp.where(i == 0, c_ref[...], x_ref[...])

        @pl.when(i == 0)
        def _():
            dnw_ref[...] = jnp.zeros_like(dnw_ref)

        @pl.when(i <= 1)
        def _():
            dss_ref[...] = jnp.zeros_like(dss_ref)

        dh_ = dh_ref[...]
        nw = nw_ref[...]
        scale = ss_ref[0][1:2]
        dxn, xh = _norm_bwd_rows(tok, dh_, nw, scale)

        @pl.when(i >= 1)
        def _():
            dx_ref[...] = dx1_ref[...] + dxn

        dsh = jnp.sum(dh_, axis=0, keepdims=True)
        dsc = jnp.sum(dh_ * xh * nw, axis=0, keepdims=True)
        dss_ref[...] += jnp.concatenate([dsh, dsc], axis=0)[None]
        dnw_ref[...] += jnp.sum(dh_ * xh * (1.0 + scale), axis=0, keepdims=True)

    row = pl.BlockSpec((TM, D), lambda i: (i, 0))
    lat = pl.BlockSpec((TM, D), lambda i: (jnp.maximum(i - 1, 0), 0))
    sel = pl.BlockSpec((1, 2, D), lambda i: (jnp.minimum(i, 1), 0, 0))
    return _pcall(body, name="mod1_bwd", grid=(tt // TM,),
                  in_specs=_tok_specs() + [row, lat, _full((1, D)), sel],
                  out_specs=[lat, sel, _full((1, D))],
                  out_shape=[jax.ShapeDtypeStruct((s_len, D), F32), jax.ShapeDtypeStruct((2, 2, D), F32),
                             jax.ShapeDtypeStruct((1, D), F32)], carry=carry)(ctx, x, dh, dx1, nw1, ss1)


def _rows(c):
    return slice(c * CH, (c + 1) * CH)


def _chunk_masks(rev, transpose=False):
    r = lax.broadcasted_iota(jnp.int32, (TM, TM), 0)
    c = lax.broadcasted_iota(jnp.int32, (TM, TM), 1)
    same = (r // CH) == (c // CH)
    before = (c >= r) if (rev != transpose) else (c <= r)
    return same & before, same


def _chunk_scan(x, rev, transpose=False):
    r = lax.broadcasted_iota(jnp.int32, (CH, CH), 0)
    c = lax.broadcasted_iota(jnp.int32, (CH, CH), 1)
    tri = ((c >= r) if (rev != transpose) else (c <= r)).astype(F32)
    return jnp.concatenate([_dot(tri, x[_rows(ch)], prec=HI) for ch in range(x.shape[0] // CH)], axis=0)


def _chunk_total(x):
    return jnp.concatenate([jnp.broadcast_to(jnp.sum(x[_rows(ch)], axis=0, keepdims=True), (CH, x.shape[1]))
                            for ch in range(x.shape[0] // CH)], axis=0)


def _hgrn_gate(fl, qraw, lg):
    lb = 1.0 / (1.0 + jnp.exp(lg[1:2] - lg[0:1]))
    sg = _sig(fl)
    f = lb + (1.0 - lb) * sg
    q = qraw * _sig(qraw) * (HGD ** -0.5)
    return lb, sg, f, q


def _hgrn_fwd(p, lg, *, rev, carry=None, readout=None):
    tt = p.shape[0]
    nt = tt // TM
    ncht = TM // CH
    d = 1 if rev else 0

    def tile_of(s):
        return jnp.where(s == 0, 0, nt - s) if rev else s

    def body(*refs):
        if readout is None:
            f_ref, inp_ref, q_ref, lg_ref, o_ref, st_ref, state = refs
        else:
            f_ref, inp_ref, q_ref, lg_ref, oo_ref, g_ref, hw_ref, o_ref, st_ref, y_ref, state = refs
        s = pl.program_id(0)

        @pl.when(s == 0)
        def _():
            state[...] = jnp.zeros_like(state)

        _, _, f, q = _hgrn_gate(f_ref[...], q_ref[...], lg_ref[0])
        lf = jnp.log(f)
        causal, _ = _chunk_masks(rev)
        cum = _chunk_scan(lf, rev)
        tot = _chunk_total(lf)
        qd = (q * jnp.exp(cum)).astype(BF16)
        kd = ((1.0 - f) * jnp.exp(-cum)).astype(BF16)
        ke = ((1.0 - f) * jnp.exp(tot - cum)).astype(BF16)
        et = jnp.exp(tot)
        v = inp_ref[...].astype(BF16)
        order = range(ncht - 1, -1, -1) if rev else range(ncht)
        outs = []
        for h in range(4):
            sl = slice(h * HGD, (h + 1) * HGD)
            qd_, kd_, ke_, v_ = qd[:, sl], kd[:, sl], ke[:, sl], v[:, sl]
            pm = jnp.where(causal, _dot(qd_, kd_, NT), 0.0).astype(BF16)
            o_h = _dot(pm, v_)
            upd = [_dot(v_[_rows(c)], ke_[_rows(c)], TN) for c in range(ncht)]
            st = state[h]
            for c in order:
                st_ref[c, h] = st
                st = st * et[c * CH:c * CH + 1, sl] + upd[c]
            state[h] = st
            inter = [_dot(qd_[_rows(c)], st_ref[c, h].astype(BF16), NT) for c in range(ncht)]
            outs.append(o_h + jnp.concatenate(inter, axis=0))
        o_tile = jnp.concatenate(outs, axis=1)
        o_ref[...] = o_tile
        if readout is not None:
            @pl.when(tile_of(s) >= 1)
            def _():
                g = g_ref[...]
                y_ref[...] = (_head_rms(oo_ref[...] + o_tile, None, 4) * hw_ref[...] * (g * _sig(g))).astype(BF16)

    def col(cb):
        return pl.BlockSpec((TM, HGW), lambda s: (tile_of(s), cb))

    in_specs = [col(C_FB if rev else C_FF), col(C_INP), col(C_QHG), pl.BlockSpec((1, 2, HGW), lambda s: (d, 0, 0))]
    out_specs = [col(0), pl.BlockSpec((ncht, 4, HGD, HGD), lambda s: (tile_of(s), 0, 0, 0))]
    out_shape = [jax.ShapeDtypeStruct((tt, HGW), F32), jax.ShapeDtypeStruct((nt * ncht, 4, HGD, HGD), F32)]
    args = [p, p, p, lg]
    if readout is not None:
        in_specs += [col(0), col(C_GHG), _full((1, HGW))]
        args += [readout[0], p, readout[1]]
        assert rev
        out_specs.append(pl.BlockSpec((TM, HGW), lambda s: (jnp.where(s == 0, nt - 2, tile_of(s) - 1), 0)))
        out_shape.append(jax.ShapeDtypeStruct((tt - L, HGW), BF16))
    return _pcall(body, name="hgrn_fwd_rev" if rev else "hgrn_fwd", grid=(nt,), in_specs=in_specs,
                  out_specs=out_specs, out_shape=out_shape, scratch=[pltpu.VMEM((4, HGD, HGD), F32)],
                  carry=carry)(*args)


def _hgrn_bwd(p, lg, do, st, dp, prev, *, rev, carry=None):
    tt = p.shape[0]
    nt = tt // TM
    ncht = TM // CH
    d = 1 if rev else 0
    second = prev is not None

    def tile_of(s):
        return jnp.where(s == nt - 1, 0, s + 1) if rev else nt - 1 - s

    def body(*refs):
        if second:
            (f_ref, inp_ref, q_ref, lg_ref, do_ref, st_ref, dvp_ref, dqp_ref, _dp_in,
             dp_ref, dlg_ref, dstate) = refs
        else:
            (f_ref, inp_ref, q_ref, lg_ref, do_ref, st_ref, _dp_in,
             dp_ref, dv_ref, dq_ref, dlg_ref, dstate) = refs
        s = pl.program_id(0)
        tile = tile_of(s)

        @pl.when(s == 0)
        def _():
            dstate[...] = jnp.zeros_like(dstate)
            dlg_ref[...] = jnp.zeros_like(dlg_ref)

        qraw = q_ref[...]
        lb, sg, f, q = _hgrn_gate(f_ref[...], qraw, lg_ref[0])
        lf = jnp.log(f)
        causal, _ = _chunk_masks(rev)
        causal_t, _ = _chunk_masks(rev, transpose=True)
        cum = _chunk_scan(lf, rev)
        tot = _chunk_total(lf)
        ea, eb, ee, et = jnp.exp(cum), jnp.exp(-cum), jnp.exp(tot - cum), jnp.exp(tot)
        qdf, kdf, kef = q * ea, (1.0 - f) * eb, (1.0 - f) * ee
        qd, kd, ke = qdf.astype(BF16), kdf.astype(BF16), kef.astype(BF16)
        v = inp_ref[...].astype(BF16)
        dob = jnp.where(tile == 0, 0.0, do_ref[...]).astype(BF16)
        order = range(ncht) if rev else range(ncht - 1, -1, -1)
        dq_l, dk_l, dv_l, dcum_l, dtot_l = [], [], [], [], []
        for h in range(4):
            sl = slice(h * HGD, (h + 1) * HGD)
            qd_, kd_, ke_, v_, do_ = qd[:, sl], kd[:, sl], ke[:, sl], v[:, sl], dob[:, sl]
            pmt = jnp.where(causal_t, _dot(kd_, qd_, NT), 0.0).astype(BF16)
            dpm = jnp.where(causal, _dot(do_, v_, NT), 0.0).astype(BF16)
            dpmt = jnp.where(causal_t, _dot(v_, do_, NT), 0.0).astype(BF16)
            dv = _dot(pmt, do_)
            dqd = _dot(dpm, kd_)
            dkd = _dot(dpmt, qd_)
            upd = [_dot(do_[_rows(c)], qd_[_rows(c)], TN) for c in range(ncht)]
            ds = dstate[h]
            ds1 = [None] * ncht
            for c in order:
                ds1[c] = ds
                ds = ds * et[c * CH:c * CH + 1, sl] + upd[c]
            dstate[h] = ds
            dke_c, dv_c, dqd_c, dtot_c = [], [], [], []
            for c in range(ncht):
                st0 = st_ref[c, h]
                dsb = ds1[c].astype(BF16)
                dke_ = _dot(v_[_rows(c)], dsb)
                dke_c.append(dke_)
                dv_c.append(_dot(ke_[_rows(c)], dsb, NT))
                dqd_c.append(_dot(do_[_rows(c)], st0.astype(BF16)))
                dt = (jnp.sum(ds1[c] * st0, axis=0, keepdims=True) * et[c * CH:c * CH + 1, sl]
                      + jnp.sum(dke_ * kef[_rows(c), sl], axis=0, keepdims=True))
                dtot_c.append(jnp.broadcast_to(dt, (CH, HGD)))
            dke = jnp.concatenate(dke_c, axis=0)
            dqd = dqd + jnp.concatenate(dqd_c, axis=0)
            dv_l.append(dv + jnp.concatenate(dv_c, axis=0))
            dtot_l.append(jnp.concatenate(dtot_c, axis=0))
            dq_l.append(dqd * ea[:, sl])
            dk_l.append(dkd * eb[:, sl] + dke * ee[:, sl])
            dcum_l.append(dqd * qdf[:, sl] - dkd * kdf[:, sl] - dke * kef[:, sl])
        dcum = jnp.concatenate(dcum_l, axis=1)
        dlf = _chunk_scan(dcum, rev, transpose=True) + jnp.concatenate(dtot_l, axis=1)
        dq_t = jnp.concatenate(dq_l, axis=1)
        dv_t = jnp.concatenate(dv_l, axis=1)

        df = dlf / f - jnp.concatenate(dk_l, axis=1)
        dfl = df * (1.0 - lb) * sg * (1.0 - sg)
        dlb = jnp.sum(df * (1.0 - sg), axis=0, keepdims=True)
        dl0 = dlb * lb * (1.0 - lb)
        dlg_ref[...] += jnp.concatenate([dl0, -dl0], axis=0)[None]
        if second:
            sq = _sig(qraw)
            dqr = (dqp_ref[...] + dq_t) * (HGD ** -0.5) * (sq * (1.0 + qraw * (1.0 - sq)))
            dp_ref[...] = jnp.concatenate([dfl, dvp_ref[...] + dv_t, dqr], axis=1).astype(BF16)
        else:
            dp_ref[...] = dfl.astype(BF16)
            dv_ref[...] = dv_t
            dq_ref[...] = dq_t

    def col(cb):
        return pl.BlockSpec((TM, HGW), lambda s: (tile_of(s), cb))

    tok = pl.BlockSpec((TM, HGW), lambda s: (tile_of(s), 0))
    in_specs = [col(C_FB if rev else C_FF), col(C_INP), col(C_QHG),
                pl.BlockSpec((1, 2, HGW), lambda s: (d, 0, 0)),
                pl.BlockSpec((TM, HGW), lambda s: (jnp.maximum(tile_of(s) - 1, 0), 0)),
                pl.BlockSpec((ncht, 4, HGD, HGD), lambda s: (tile_of(s), 0, 0, 0))]
    args = [p, p, p, lg, do, st]
    dlg_spec = _full((1, 2, HGW))
    dlg_shape = jax.ShapeDtypeStruct((1, 2, HGW), F32)
    if second:
        in_specs += [tok, tok]
        args += [prev[0], prev[1]]
        out_specs = [pl.BlockSpec((TM, 3 * HGW), lambda s: (tile_of(s), 0)), dlg_spec]
        out_shape = [jax.ShapeDtypeStruct(dp.shape, BF16), dlg_shape]
    else:
        out_specs = [pl.BlockSpec((TM, HGW), lambda s: (tile_of(s), C_FB if rev else C_FF)), tok, tok, dlg_spec]
        out_shape = [jax.ShapeDtypeStruct(dp.shape, BF16), jax.ShapeDtypeStruct((tt, HGW), F32),
                     jax.ShapeDtypeStruct((tt, HGW), F32), dlg_shape]
    in_specs.append(ANY)
    args.append(dp)
    return _pcall(body, name="hgrn_bwd_rev" if rev else "hgrn_bwd", grid=(nt,),
                  in_specs=in_specs, out_specs=out_specs, out_shape=out_shape,
                  scratch=[pltpu.VMEM((4, HGD, HGD), F32)],
                  aliases={len(args) - 1: 0}, carry=carry)(*args)


def _head_rms(o, w, nheads):
    outs = []
    for h in range(nheads):
        oh = o[:, h * HGD:(h + 1) * HGD]
        outs.append(oh * lax.rsqrt(jnp.mean(oh * oh, axis=-1, keepdims=True) + EPS))
    return jnp.concatenate(outs, axis=1)


def _readout(o0, o1, p, hw4):
    s_len = o0.shape[0] - L

    def body(o0_ref, o1_ref, g_ref, w_ref, y_ref):
        xh = _head_rms(o0_ref[...] + o1_ref[...], None, 4)
        g = g_ref[...]
        y_ref[...] = (xh * w_ref[...] * (g * _sig(g))).astype(BF16)

    lat = pl.BlockSpec((TM, HGW), lambda i: (i + 1, 0))
    return _pcall(body, name="readout", grid=(s_len // TM,),
                  in_specs=[lat, lat, pl.BlockSpec((TM, HGW), lambda i: (i + 1, C_GHG)), _full((1, HGW))],
                  out_specs=pl.BlockSpec((TM, HGW), lambda i: (i, 0)),
                  out_shape=jax.ShapeDtypeStruct((s_len, HGW), BF16))(o0, o1, p, hw4)


def _readout_bwd(o0, o1, p, hw4, dy, dp, carry=None):
    tt = o0.shape[0]
    s_len = tt - L

    def body(o0_ref, o1_ref, g_ref, w_ref, dy_ref, _dp_in, dp_ref, do_ref, dw_ref):
        i = pl.program_id(0)

        @pl.when(i == 0)
        def _():
            dw_ref[...] = jnp.zeros_like(dw_ref)
            dp_ref[...] = jnp.zeros_like(dp_ref)

        @pl.when(i >= 1)
        def _():
            o = o0_ref[...] + o1_ref[...]
            g = g_ref[...]
            w = w_ref[...]
            sg = _sig(g)
            dy_ = dy_ref[...]
            dsw = dy_ * (g * sg)
            outs, xhs = [], []
            for h in range(4):
                sl = slice(h * HGD, (h + 1) * HGD)
                oh = o[:, sl]
                r = lax.rsqrt(jnp.mean(oh * oh, axis=-1, keepdims=True) + EPS)
                xh = oh * r
                dxh = dsw[:, sl] * w[:, sl]
                outs.append(r * (dxh - xh * jnp.mean(dxh * xh, axis=-1, keepdims=True)))
                xhs.append(xh)
            xh = jnp.concatenate(xhs, axis=1)
            do_ref[...] = jnp.concatenate(outs, axis=1)
            dp_ref[...] = (dy_ * xh * w * (sg * (1.0 + g * (1.0 - sg)))).astype(BF16)
            dw_ref[...] += jnp.sum(dsw * xh, axis=0, keepdims=True)

    tok = pl.BlockSpec((TM, HGW), lambda i: (i, 0))
    lat = pl.BlockSpec((TM, HGW), lambda i: (jnp.maximum(i - 1, 0), 0))
    return _pcall(body, name="readout_bwd", grid=(tt // TM,),
                  in_specs=[tok, tok, pl.BlockSpec((TM, HGW), lambda i: (i, C_GHG)), _full((1, HGW)), lat, ANY],
                  out_specs=[pl.BlockSpec((TM, HGW), lambda i: (i, C_GHG)), lat, _full((1, HGW))],
                  out_shape=[jax.ShapeDtypeStruct(dp.shape, BF16), jax.ShapeDtypeStruct((s_len, HGW), F32),
                             jax.ShapeDtypeStruct((1, HGW), F32)],
                  aliases={5: 0}, carry=carry)(o0, o1, p, hw4, dy, dp)


def _rope_tables(s_len):
    t = np.arange(s_len)
    inv = ROPE_THETA ** (-np.arange(0, 32, 2, dtype=np.float64) / 32)
    def half(pos):
        ang = pos[:, None].astype(np.float64) * inv[None, :]
        return (np.concatenate([np.cos(ang), np.cos(ang)], 1), np.concatenate([-np.sin(ang), np.sin(ang)], 1))
    cr, sr = half(t // GRID_W)
    cc, sc = half(t % GRID_W)
    cos = np.concatenate([cr, cc, cr, cc], 1)
    sin = np.concatenate([sr, sc, sr, sc], 1)
    cos = np.concatenate([np.ones((L, 128)), cos], 0)
    sin = np.concatenate([np.zeros((L, 128)), sin], 0)
    return jnp.asarray(cos, F32), jnp.asarray(sin, F32)


def _blockdiag(n, w):
    i = np.arange(n)
    return jnp.asarray((i[:, None] // w == i[None, :] // w) / float(w), F32)


def _dup_matrix():
    m = np.zeros((128, 512), np.float32)
    for g in range(2):
        for j in range(4):
            for dd in range(HDIM):
                m[64 * g + dd, 256 * g + 64 * j + dd] = 1.0
    return m


def _head_mean(x, blockdiag):
    return _dot(x, blockdiag, prec=lax.Precision.HIGH)


def _rot(x):
    n = x.shape[1]
    lane = lax.broadcasted_iota(jnp.int32, x.shape, 1)
    return jnp.where((lane % 32) < 16, pltpu.roll(x, n - 16, 1), pltpu.roll(x, 16, 1))


def _qk_prep(p, cos, sin, qnw8, knw2, bd512, bd128, dup):
    tt = p.shape[0]

    def body(q_ref, kv_ref, cos_ref, sin_ref, qw_ref, kw_ref, b5_ref, b1_ref, dup_ref,
             qr_ref, k4_ref, v4_ref):
        cos_, sin_ = cos_ref[...], sin_ref[...]
        q = q_ref[...]
        qn = q * lax.rsqrt(_head_mean(q * q, b5_ref[...]) + EPS) * qw_ref[...]
        cos4 = jnp.concatenate([cos_] * 4, axis=1)
        sin4 = jnp.concatenate([sin_] * 4, axis=1)
        qr_ref[...] = ((qn * cos4 + _rot(qn) * sin4) * (HDIM ** -0.5)).astype(BF16)
        kv = kv_ref[...]
        k, v = kv[:, :128], kv[:, 128:]
        kn = k * lax.rsqrt(_head_mean(k * k, b1_ref[...]) + EPS) * kw_ref[...]
        kr = kn * cos_ + _rot(kn) * sin_
        k4_ref[...] = _bdot(kr, dup_ref[...]).astype(BF16)
        v4_ref[...] = _bdot(v, dup_ref[...]).astype(BF16)

    row = lambda w, cb: pl.BlockSpec((TM, w), lambda i: (i, cb))
    out = jax.ShapeDtypeStruct((tt, ATW), BF16)
    return _pcall(body, name="qk_prep", grid=(tt // TM,),
                  in_specs=[row(ATW, C_QRAW), row(256, C_KV), row(128, 0), row(128, 0),
                            _full((1, ATW)), _full((1, 128)), _full((ATW, ATW)), _full((128, 128)),
                            _full((128, ATW))],
                  out_specs=[row(ATW, 0)] * 3, out_shape=[out] * 3)(
                      p, p, cos, sin, qnw8, knw2, bd512, bd128, dup)


def _attn_masks(i, nb):
    r = lax.broadcasted_iota(jnp.int32, (4 * BLK, 3 * BLK + L), 0) % BLK
    c = lax.broadcasted_iota(jnp.int32, (4 * BLK, 3 * BLK + L), 1)
    kpos = (i - 1) * BLK + c
    loc = (jnp.abs(c - BLK - r) <= BLK) & (kpos >= 0) & (kpos < nb * BLK)
    return loc | (c >= 3 * BLK)


def _stack_mask():
    r = lax.broadcasted_iota(jnp.int32, (4 * BLK, 256), 0)
    lane = lax.broadcasted_iota(jnp.int32, (4 * BLK, 256), 1)
    return (r // BLK) == (lane // HDIM)


def _stack_heads(xg, fill=0.0):
    x4 = jnp.concatenate([xg] * 4, axis=0)
    return jnp.where(_stack_mask(), x4, jnp.full_like(x4, fill))


def _unstack_heads(x4):
    out = jnp.where(_lane_mask(0), x4[0:BLK], 0.0)
    for j in range(1, 4):
        out = out + jnp.where(_lane_mask(j), x4[j * BLK:(j + 1) * BLK], 0.0)
    return out


def _per_head_rows(vals):
    return jnp.concatenate([jnp.broadcast_to(v, (BLK, 1)) for v in vals], axis=0)


def _lane_mask(j):
    lane = lax.broadcasted_iota(jnp.int32, (1, 256), 1)
    return (lane // HDIM) == j


def _attn_specs(nb):
    blk = lambda off: pl.BlockSpec((BLK, ATW), lambda i: (jnp.clip(i + off, 0, nb - 1) + 2, 0))
    ctx = pl.BlockSpec((L, ATW), lambda i: (0, 0))
    return blk, ctx


def _attn_fwd(qr, k4, v4, sinks, carry=None):
    tt = qr.shape[0]
    s_len = tt - L
    nb = s_len // BLK

    def body(sk_ref, q_ref, kp, ko, kn, kc, vp, vo, vn, vc, y_ref, lse_ref):
        i = pl.program_id(0)
        valid = _attn_masks(i, nb)
        q = q_ref[...]
        ys, lses = [], []
        for g in range(2):
            gs = slice(256 * g, 256 * g + 256)
            kcat = jnp.concatenate([kp[:, gs], ko[:, gs], kn[:, gs], kc[:, gs]], axis=0)
            vcat = jnp.concatenate([vp[:, gs], vo[:, gs], vn[:, gs], vc[:, gs]], axis=0)
            sink4 = _per_head_rows([sk_ref[4 * g + j] for j in range(4)])
            q4 = _stack_heads(q[:, gs])
            o_parts, l_parts = [], []
            for hp in range(2):
                rows = slice(2 * BLK * hp, 2 * BLK * (hp + 1))
                sink = sink4[rows]
                s = jnp.where(valid[rows], _dot(q4[rows], kcat, NT), -1e30)
                m = jnp.maximum(jnp.max(s, axis=-1, keepdims=True), sink)
                e = jnp.exp(s - m)
                den = jnp.sum(e, axis=-1, keepdims=True) + jnp.exp(sink - m)
                o_parts.append(_bdot(e * (1.0 / den), vcat))
                l_parts.append(jnp.broadcast_to(m + jnp.log(den), (2 * BLK, 256)))
            ys.append(_unstack_heads(jnp.concatenate(o_parts, axis=0)))
            lses.append(_unstack_heads(jnp.concatenate(l_parts, axis=0)))
        y_ref[...] = jnp.concatenate(ys, axis=1).astype(BF16)
        lse_ref[...] = jnp.concatenate(lses, axis=1)

    blk, ctx = _attn_specs(nb)
    out = pl.BlockSpec((BLK, ATW), lambda i: (i, 0))
    return _pcall(body, name="attn_fwd", grid=(nb,),
                  in_specs=[pl.BlockSpec(memory_space=pltpu.SMEM), blk(0),
                            blk(-1), blk(0), blk(1), ctx, blk(-1), blk(0), blk(1), ctx],
                  out_specs=[out, out],
                  out_shape=[jax.ShapeDtypeStruct((s_len, ATW), BF16),
                             jax.ShapeDtypeStruct((s_len, ATW), F32)], carry=carry)(
                      sinks, qr, k4, k4, k4, k4, v4, v4, v4, v4)


def _attn_bwd(qr, k4, v4, sinks, y, lse, dy, carry=None):
    tt = qr.shape[0]
    s_len = tt - L
    nb = s_len // BLK

    def body(sk_ref, q_ref, kp, ko, kn, kc, vp, vo, vn, vc, y_ref, lse_ref, dy_ref,
             dq_ref, dkw_ref, dvw_ref, dkc_ref, dvc_ref, dsk_ref):
        i = pl.program_id(0)

        @pl.when(i == 0)
        def _():
            dkc_ref[...] = jnp.zeros_like(dkc_ref)
            dvc_ref[...] = jnp.zeros_like(dvc_ref)
            dsk_ref[...] = jnp.zeros_like(dsk_ref)

        valid = _attn_masks(i, nb)
        q = q_ref[...]
        dy_ = dy_ref[...]
        dly = dy_ * y_ref[...].astype(F32)
        lse_ = lse_ref[...]
        dqs = []
        for g in range(2):
            gs = slice(256 * g, 256 * g + 256)
            kcat = jnp.concatenate([kp[:, gs], ko[:, gs], kn[:, gs], kc[:, gs]], axis=0)
            vcat = jnp.concatenate([vp[:, gs], vo[:, gs], vn[:, gs], vc[:, gs]], axis=0)
            q4 = _stack_heads(q[:, gs])
            dy4 = _stack_heads(dy_[:, gs]).astype(BF16)
            lse4 = jnp.max(_stack_heads(lse_[:, gs], fill=-1e30), axis=-1, keepdims=True)
            delta = jnp.sum(_stack_heads(dly[:, gs]), axis=-1, keepdims=True)
            sink = _per_head_rows([sk_ref[4 * g + j] for j in range(4)])
            pr = jnp.where(valid, jnp.exp(_dot(q4, kcat, NT) - lse4), 0.0)
            dsb = (pr * (_dot(dy4, vcat, NT) - delta)).astype(BF16)
            dsink = jnp.exp(sink - lse4) * delta
            for j in range(4):
                dsk_ref[4 * g + j:4 * g + j + 1, :] += jnp.broadcast_to(
                    -jnp.sum(dsink[j * BLK:(j + 1) * BLK], axis=0, keepdims=True), (1, 128))
            dqs.append(_unstack_heads(_dot(dsb, kcat)))
            dkg = _dot(dsb, q4, TN)
            dvg = _dot(pr.astype(BF16), dy4, TN)
            dkw_ref[0, :, gs] = dkg[:3 * BLK]
            dvw_ref[0, :, gs] = dvg[:3 * BLK]
            dkc_ref[:, gs] += dkg[3 * BLK:]
            dvc_ref[:, gs] += dvg[3 * BLK:]
        dq_ref[...] = jnp.concatenate(dqs, axis=1)

    blk, ctx = _attn_specs(nb)
    out = pl.BlockSpec((BLK, ATW), lambda i: (i, 0))
    win = pl.BlockSpec((1, 3 * BLK, ATW), lambda i: (i, 0, 0))
    acc = _full((L, ATW))
    return _pcall(body, name="attn_bwd", grid=(nb,),
                  in_specs=[pl.BlockSpec(memory_space=pltpu.SMEM), blk(0),
                            blk(-1), blk(0), blk(1), ctx, blk(-1), blk(0), blk(1), ctx, out, out, out],
                  out_specs=[out, win, win, acc, acc, _full((8, 128))],
                  out_shape=[jax.ShapeDtypeStruct((s_len, ATW), F32),
                             jax.ShapeDtypeStruct((nb, 3 * BLK, ATW), F32),
                             jax.ShapeDtypeStruct((nb, 3 * BLK, ATW), F32),
                             jax.ShapeDtypeStruct((L, ATW), F32), jax.ShapeDtypeStruct((L, ATW), F32),
                             jax.ShapeDtypeStruct((8, 128), F32)], carry=carry)(
                      sinks, qr, k4, k4, k4, k4, v4, v4, v4, v4, y, lse, dy)


def _attn_post(p, cos, sin, qnw8, knw2, bd512, bd128, dupt, dq, dkw, dvw, dkc, dvc, dp, carry=None):
    tt = p.shape[0]
    s_len = tt - L
    nb = s_len // BLK
    nctx = L // BLK

    def body(q_ref, kv_ref, cos_ref, sin_ref, qw_ref, kw_ref, b5_ref, b1_ref, dupt_ref,
             dq_ref, kwp, kwo, kwn, vwp, vwo, vwn, dkc_ref, dvc_ref, _dp_in,
             dp_ref, dqw_ref, dkw_ref):
        t = pl.program_id(0)
        j = t - nctx

        @pl.when(t == 0)
        def _():
            dqw_ref[...] = jnp.zeros_like(dqw_ref)
            dkw_ref[...] = jnp.zeros_like(dkw_ref)

        is_lat = t >= nctx
        cos_, sin_ = cos_ref[...], sin_ref[...]
        has_p = is_lat & (j >= 1)
        has_n = is_lat & (j <= nb - 2)
        dk4 = (jnp.where(is_lat, kwo[0], dkc_ref[...]) + jnp.where(has_p, kwp[0], 0.0)
               + jnp.where(has_n, kwn[0], 0.0))
        dv4 = (jnp.where(is_lat, vwo[0], dvc_ref[...]) + jnp.where(has_p, vwp[0], 0.0)
               + jnp.where(has_n, vwn[0], 0.0))
        dkr = _dot(dk4, dupt_ref[...], prec=HI)
        dv = _dot(dv4, dupt_ref[...], prec=HI)
        kv = kv_ref[...]
        k = kv[:, :128]
        kw = kw_ref[...]
        rk = lax.rsqrt(_head_mean(k * k, b1_ref[...]) + EPS)
        xk = k * rk
        dkn = dkr * cos_ + _rot(dkr * sin_)
        dxk = dkn * kw
        dk = rk * (dxk - xk * _head_mean(dxk * xk, b1_ref[...]))
        dkw_ref[...] += jnp.sum(dkn * xk, axis=0, keepdims=True)
        q = q_ref[...]
        qw = qw_ref[...]
        rq = lax.rsqrt(_head_mean(q * q, b5_ref[...]) + EPS)
        xq = q * rq
        cos4 = jnp.concatenate([cos_] * 4, axis=1)
        sin4 = jnp.concatenate([sin_] * 4, axis=1)
        dqr = jnp.where(is_lat, dq_ref[...], 0.0) * (HDIM ** -0.5)
        dqn = dqr * cos4 + _rot(dqr * sin4)
        dxq = dqn * qw
        dqraw = rq * (dxq - xq * _head_mean(dxq * xq, b5_ref[...]))
        dqw_ref[...] += jnp.sum(dqn * xq, axis=0, keepdims=True)
        dp_ref[...] = jnp.concatenate([dqraw, dk, dv], axis=1).astype(BF16)

    row = lambda w, cb: pl.BlockSpec((BLK, w), lambda t: (t, cb))
    lat = pl.BlockSpec((BLK, ATW), lambda t: (jnp.maximum(t - nctx, 0), 0))

    def part(off):
        return pl.BlockSpec((1, BLK, ATW), lambda t: (jnp.clip(t - nctx + off, 0, nb - 1), 1 - off, 0))

    cacc = pl.BlockSpec((BLK, ATW), lambda t: (jnp.minimum(t, nctx - 1), 0))
    return _pcall(body, name="attn_post", grid=(tt // BLK,),
                  in_specs=[row(ATW, C_QRAW), row(256, C_KV), row(128, 0), row(128, 0),
                            _full((1, ATW)), _full((1, 128)), _full((ATW, ATW)), _full((128, 128)),
                            _full((ATW, 128)), lat, part(-1), part(0), part(1), part(-1), part(0), part(1),
                            cacc, cacc, ANY],
                  out_specs=[pl.BlockSpec((BLK, 768), lambda t: (t, C_QKV)), _full((1, ATW)), _full((1, 128))],
                  out_shape=[jax.ShapeDtypeStruct(dp.shape, BF16), jax.ShapeDtypeStruct((1, ATW), F32),
                             jax.ShapeDtypeStruct((1, 128), F32)],
                  aliases={18: 0}, carry=carry)(p, p, cos, sin, qnw8, knw2, bd512, bd128, dupt,
                                   dq, dkw, dkw, dkw, dvw, dvw, dvw, dkc, dvc, dp)


def _branch_merge(y_hg, y_at, bh4, ba4, p):
    s_len = y_hg.shape[0]

    def body(yh_ref, ya_ref, bh_ref, ba_ref, gh_ref, ga_ref, ah_ref, aa_ref, m_ref):
        yh, ya = yh_ref[...], ya_ref[...]
        ah = jnp.concatenate([_bdot(yh, bh_ref[j]) for j in range(4)], axis=1)
        aa = jnp.concatenate([_bdot(ya, ba_ref[j]) for j in range(4)], axis=1)
        ah_ref[...] = ah.astype(BF16)
        aa_ref[...] = aa.astype(BF16)
        m_ref[...] = (_sig(gh_ref[...]) * ah + _sig(ga_ref[...]) * aa).astype(BF16)

    row = pl.BlockSpec((TM, D), lambda i: (i, 0))
    y = pl.BlockSpec((TM, HGW), lambda i: (i, 0))
    f = jax.ShapeDtypeStruct((s_len, D), BF16)
    return _pcall(body, name="branch_merge", grid=(s_len // TM,),
                  in_specs=[y, y, _full(bh4.shape), _full(ba4.shape),
                            pl.BlockSpec((TM, D), lambda i: (i + 1, 2)), pl.BlockSpec((TM, D), lambda i: (i + 1, 3))],
                  out_specs=[row, row, row],
                  out_shape=[f, f, jax.ShapeDtypeStruct((s_len, D), BF16)])(y_hg, y_at, bh4, ba4, p, p)


def _branch_bwd(dmh, dma, bh4, ba4, y_hg, y_at):
    s_len = dmh.shape[0]
    nk = s_len // TS
    ns = D // 4

    def body(dh_ref, da_ref, bh_ref, ba_ref, yh_ref, ya_ref, dyh_ref, dya_ref, gh_ref, ga_ref, acc_h, acc_a):
        t = pl.program_id(0)

        @pl.when(t == 0)
        def _():
            acc_h[...] = jnp.zeros_like(acc_h)
            acc_a[...] = jnp.zeros_like(acc_a)

        for d_ref, w_ref, y_ref, dy_ref, acc in ((dh_ref, bh_ref, yh_ref, dyh_ref, acc_h),
                                                 (da_ref, ba_ref, ya_ref, dya_ref, acc_a)):
            y = y_ref[...]
            dy = jnp.zeros((TS, HGW), F32)
            for j in range(4):
                dj = d_ref[:, j * ns:(j + 1) * ns]
                dy = dy + _bdot(dj, w_ref[j], NT)
                acc[j] += _bdot(y, dj, TN)
            dy_ref[...] = dy

        @pl.when(t == nk - 1)
        def _():
            gh_ref[...] = acc_h[...].astype(BF16)
            ga_ref[...] = acc_a[...].astype(BF16)

    dm = pl.BlockSpec((TS, D), lambda t: (t, 0))
    y = pl.BlockSpec((TS, HGW), lambda t: (t, 0))
    w = _full(bh4.shape)
    fy = jax.ShapeDtypeStruct((s_len, HGW), F32)
    gw = jax.ShapeDtypeStruct(bh4.shape, BF16)
    return _pcall(body, name="branch_bwd", grid=(nk,), in_specs=[dm, dm, w, w, y, y],
                  out_specs=[y, y, w, w], out_shape=[fy, fy, gw, gw],
                  scratch=[pltpu.VMEM(bh4.shape, F32)] * 2)(dmh, dma, bh4, ba4, y_hg, y_at)


def _merge_bwd(dattn, w_o, mixed, ah, aa, p, carry=None):
    tt = p.shape[0]
    s_len = tt - L
    nt = tt // TM

    def body(da_ref, wo_ref, mx_ref, ah_ref, aa_ref, gh_ref, ga_ref, dp_ref, dmh_ref, dma_ref, go_ref, acc):
        i = pl.program_id(0)

        @pl.when(i == 0)
        def _():
            dp_ref[...] = jnp.zeros_like(dp_ref)
            acc[...] = jnp.zeros_like(acc)

        @pl.when(i >= 1)
        def _():
            da = da_ref[...]
            acc[...] += _bdot(mx_ref[...], da, TN)
            dm_ = _bdot(da, wo_ref[...], NT)
            sh, sa = _sig(gh_ref[...]), _sig(ga_ref[...])
            dp_ref[...] = jnp.concatenate([dm_ * ah_ref[...].astype(F32) * sh * (1.0 - sh),
                                           dm_ * aa_ref[...].astype(F32) * sa * (1.0 - sa)], axis=1).astype(BF16)
            dmh_ref[...] = (dm_ * sh).astype(BF16)
            dma_ref[...] = (dm_ * sa).astype(BF16)

        @pl.when(i == nt - 1)
        def _():
            go_ref[...] = acc[...].astype(BF16)

    lat = pl.BlockSpec((TM, D), lambda i: (jnp.maximum(i - 1, 0), 0))
    return _pcall(body, name="merge_bwd", grid=(nt,),
                  in_specs=[lat, _full((D, D)), lat, lat, lat, pl.BlockSpec((TM, D), lambda i: (i, 2)),
                            pl.BlockSpec((TM, D), lambda i: (i, 3))],
                  out_specs=[pl.BlockSpec((TM, 2 * D), lambda i: (i, C_GATES)), lat, lat, _full((D, D))],
                  out_shape=[jax.ShapeDtypeStruct((tt, NCOL), BF16), jax.ShapeDtypeStruct((s_len, D), BF16),
                             jax.ShapeDtypeStruct((s_len, D), BF16), jax.ShapeDtypeStruct((D, D), BF16)],
                  scratch=[pltpu.VMEM((D, D), F32)], carry=carry)(dattn, w_o, mixed, ah, aa, p, p)


def _local_step(x, ctx, tgt, mod, modc, nw1, nw2, lg, hw, qnw, knw, sinks,
                w_in, wts, dist=None):
    s_len = x.shape[0]
    tt = s_len + L
    ss1 = jnp.stack([modc, mod[0:2]])
    ss2 = mod[3:5][None]
    g1, g2 = mod[2:3], mod[5:6]
    hw4 = jnp.tile(hw, (1, 4))
    qnw8 = jnp.tile(qnw, (1, 8))
    knw2 = jnp.tile(knw, (1, 2))
    cos, sin = _rope_tables(s_len)
    bd512, bd128 = _blockdiag(ATW, HDIM), _blockdiag(128, HDIM)
    dupm = _dup_matrix()
    dup, dupt = jnp.asarray(dupm, BF16), jnp.asarray(dupm.T, F32)
    tmt = tt

    def four(b):
        return b.reshape(4, 2 * b.shape[1], b.shape[2])

    def halves(g):
        return g.reshape(4, 2, g.shape[1] // 2, g.shape[2])

    h = _mod1(ctx, x, nw1, ss1)
    if dist is None:
        bh4, ba4, w_o, g4, u4, dn4 = wts
        p = _mm_in(h, w_in, tmt)
        o0, st0 = _hgrn_fwd(p, lg, rev=False)
        o1, st1, y_hg = _hgrn_fwd(p, lg, rev=True, readout=(o0, hw4))
    else:
        core, chip = dist
        half = wts[3].shape[1] // 2
        p, first = _mm_in(h, w_in, tmt, carry=_carry_join(_carry_gather(list(wts[0:3])),
                                                          _carry_gather([wts[3]], rows=[(0, half)])))
        (o0, st0), (g8,) = _hgrn_fwd(p, lg, rev=False, carry=_carry_gather([first[3]], rows=[(half, half)]))
        (o1, st1, y_hg), (dn8a,) = _hgrn_fwd(p, lg, rev=True, readout=(o0, hw4),
                                             carry=_carry_gather([wts[5]], rows=[(0, half)]))
        bh4, ba4, w_o, g4 = four(first[0]), four(first[1]), four(first[2]).reshape(D, D), four(g8)
    qr, k4, v4 = _qk_prep(p, cos, sin, qnw8, knw2, bd512, bd128, dup)
    if dist is None:
        y_at, lse = _attn_fwd(qr, k4, v4, sinks)
    else:
        (y_at, lse), (u8, dn8) = _attn_fwd(qr, k4, v4, sinks,
                                           carry=_carry_gather([wts[4], dn8a], rows=[None, (half, half)]))
        u4, dn4 = four(u8), four(dn8)
    ah, aa, mixed = _branch_merge(y_hg, y_at, bh4, ba4, p)
    ao, x1, h2 = _out_proj_mod2(mixed, w_o, x, g1, nw2, ss2)
    a4, b4, z4 = _ffn_up(h2, g4, u4)
    sq, dx2, dyb, dg2 = _ffn_down_loss(z4, dn4, x1, g2, tgt)

    da4, db4 = _ffn_dz(dyb, dn4, a4, b4)
    g_dn = _ffn_gdn(z4, dyb)
    if dist is None:
        dh2 = _ffn_dh2(da4, db4, g4, u4)
    else:
        dn_units = [halves(g_dn)]
        dh2, dn_recv = _ffn_dh2(da4, db4, g4, u4, carry=_carry_pairx(dn_units))
        dn_pairs = _rs_pair_add(dn_units, dn_recv, core)
    if dist is None:
        g_g, g_u = _ffn_ggu(h2, da4, db4)
    else:
        (g_g, g_u), c_dn = _ffn_ggu(h2, da4, db4, carry=_carry_chipx(dn_pairs))
        red_dn = _rs_chip_add(dn_pairs, c_dn, core, chip)
    dx1, dattn, dss2, dnw2, dg1 = _mod2_bwd(x1, dh2, dx2, ao, nw2, ss2, g1)
    if dist is None:
        dp, dmh, dma, g_o = _merge_bwd(dattn, w_o, mixed, ah, aa, p)
    else:
        gu_units = [halves(g_g), halves(g_u)]
        (dp, dmh, dma, g_o), gu_recv = _merge_bwd(dattn, w_o, mixed, ah, aa, p, carry=_carry_pairx(gu_units))
        ffn_pairs = list(dn_pairs) + list(_rs_pair_add(gu_units, gu_recv, core))
    dy_hg, dy_at, g_bh, g_ba = _branch_bwd(dmh, dma, bh4, ba4, y_hg, y_at)
    if dist is None:
        dp, do, dhw4 = _readout_bwd(o0, o1, p, hw4, dy_hg, dp)
        dq, dkw, dvw, dkc, dvc, dsk = _attn_bwd(qr, k4, v4, sinks, y_at, lse, dy_at)
        dp, dqnw8, dknw2 = _attn_post(p, cos, sin, qnw8, knw2, bd512, bd128, dupt, dq, dkw, dvw, dkc, dvc, dp)
    else:
        mix_units = [halves(g_bh), halves(g_ba), halves(g_o.reshape(4, D // 4, D))]
        (dp, do, dhw4), mix_recv = _readout_bwd(o0, o1, p, hw4, dy_hg, dp, carry=_carry_pairx(mix_units))
        mix_pairs = _rs_pair_add(mix_units, mix_recv, core)
        (dq, dkw, dvw, dkc, dvc, dsk), bwd = _attn_bwd(
            qr, k4, v4, sinks, y_at, lse, dy_at,
            carry=_carry_join(_carry_chipx(ffn_pairs[1:2]), _carry_sibx(red_dn)))
        red_g = _rs_chip_add(ffn_pairs[1:2], bwd[0:1], core, chip)
        (dp, dqnw8, dknw2), post = _attn_post(
            p, cos, sin, qnw8, knw2, bd512, bd128, dupt, dq, dkw, dvw, dkc, dvc, dp, carry=_carry_sibx(red_g))
    if dist is None:
        dp, dv0, dq0, dlg0 = _hgrn_bwd(p, lg, do, st0, dp, None, rev=False)
        dp, dlg1 = _hgrn_bwd(p, lg, do, st1, dp, (dv0, dq0), rev=True)
    else:
        (dp, dv0, dq0, dlg0), c_u = _hgrn_bwd(p, lg, do, st0, dp, None, rev=False,
                                              carry=_carry_chipx(ffn_pairs[2:3]))
        red_u = _rs_chip_add(ffn_pairs[2:3], c_u, core, chip)
        (dp, dlg1), last = _hgrn_bwd(p, lg, do, st1, dp, (dv0, dq0), rev=True,
                                     carry=_carry_join(_carry_chipx(mix_pairs), _carry_sibx(red_u)))
        mix_reds = _rs_chip_add(mix_pairs, last[0:3], core, chip)
        ffn_done = bwd[1:2] + post[0:1] + last[3:4]
    g_in = _mm_gin(dp, h, tmt)
    if dist is None:
        dh = _mm_dh(dp, w_in, tmt)
        gx, dss1, dnw1 = _mod1_bwd(ctx, x, dh, dx1, nw1, ss1)
        rs = None
    else:
        in_units = [halves(g_in.reshape(4, NCOL // 4, D))]
        dh, both = _mm_dh(dp, w_in, tmt, carry=_carry_join(_carry_pairx(in_units), _carry_sibx(mix_reds)))
        in_recv, mix_done = both[0:1], both[1:4]
        in_pairs = _rs_pair_add(in_units, in_recv, core)
        first_rows = (0, in_pairs[0].shape[1] // 2)
        (gx, dss1, dnw1), in_part = _mod1_bwd(ctx, x, dh, dx1, nw1, ss1,
                                              carry=_carry_chipx(in_pairs, rows=first_rows))
        rs = dict(ffn_done=ffn_done, mix_done=mix_done, in_pairs=in_pairs, in_part=in_part)

    dmod = jnp.concatenate([dss1[1], dg1, dss2, dg2], axis=0)
    dmodc = dss1[0]
    raw = (dss1, dg1, dss2, dg2, dnw1, dnw2, dhw4, dqnw8, dknw2, dsk, dlg0, dlg1)
    small = dict(raw=raw, dmod=dmod, dmodc=dmodc, dnw1=dnw1, dnw2=dnw2,
                 dhw=dhw4.reshape(4, HGD).sum(0, keepdims=True),
                 dqnw=dqnw8.reshape(8, HDIM).sum(0, keepdims=True),
                 dknw=dknw2.reshape(2, HDIM).sum(0, keepdims=True),
                 dsinks=dsk[:, 0], dlg=jnp.concatenate([dlg0, dlg1], axis=0))
    big = dict(w_in=g_in, w_bh=g_bh, w_ba=g_ba, w_o=g_o, w_g=g_g, w_u=g_u, w_dn=g_dn)
    return sq, gx, big, small, rs


def _place():
    x, y, c = lax.axis_index("x"), lax.axis_index("y"), lax.axis_index("c")
    return x, y, c


def _gather_blocks(x_refs, out_refs, send_sems, recv_sems, local_sems):
    n = len(out_refs)
    x, y, c = _place()
    me, sibling = (x, y, c), (x, y, 1 - c)
    chips = [(1 - x, y), (x, 1 - y), (1 - x, 1 - y)]

    def slot(u, px, py, pc):
        return out_refs[u].at[4 * px + 2 * py + pc]

    def copy(u, k, block, to, src=None):
        return pltpu.make_async_remote_copy(
            src_ref=slot(u, *block) if src is None else src, dst_ref=slot(u, *block),
            send_sem=send_sems.at[u, k], recv_sem=recv_sems.at[u, k], device_id=to, device_id_type=MESH)

    mines = []
    if x_refs is not None:
        mines = [pltpu.make_async_copy(x_refs[u], slot(u, *me), local_sems.at[u]) for u in range(n)]
    for cp in mines:
        cp.start()
    first = []
    for u in range(n):
        src = None if x_refs is None else x_refs[u]
        first.append(copy(u, 0, me, sibling, src=src))
        first += [copy(u, 1 + j, me, (*chip, c), src=src) for j, chip in enumerate(chips)]
    for cp in first:
        cp.start()
    passed = []
    for j, chip in enumerate(chips):
        for u in range(n):
            copy(u, 1 + j, (*chip, c), me).wait_recv()
            fwd = copy(u, 4 + j, (*chip, c), sibling)
            fwd.start()
            passed.append(fwd)
    for u in range(n):
        copy(u, 0, sibling, me).wait_recv()
    for j, chip in enumerate(chips):
        for u in range(n):
            copy(u, 4 + j, (*chip, 1 - c), me).wait_recv()
    for cp in first + passed:
        cp.wait_send()
    for cp in mines:
        cp.wait()


def _gather_sems(n):
    return [pltpu.SemaphoreType.DMA((n, 7)), pltpu.SemaphoreType.DMA((n, 7)), pltpu.SemaphoreType.DMA((n,))]


def _allgather(blks, *, name, in_vmem):
    n = len(blks)
    space = pltpu.VMEM if in_vmem else pl.ANY

    def body(*refs):
        _gather_blocks(refs[:n], refs[n:2 * n], *refs[2 * n:])

    return pl.pallas_call(
        body, name=name, out_shape=[jax.ShapeDtypeStruct((8,) + b.shape, b.dtype) for b in blks],
        in_specs=[pl.BlockSpec(memory_space=space)] * n, out_specs=[pl.BlockSpec(memory_space=space)] * n,
        scratch_shapes=_gather_sems(n))(*blks)


def _cast_place(ws, c, dev):
    n = len(ws)

    def body(s_ref, *refs):
        for u in range(n):
            refs[n + u][0] = refs[u][...].astype(BF16)

    in_specs, out_specs, out_shape = [], [], []
    for w in ws:
        q, cols = w.shape[0] // 4, w.shape[1]
        in_specs.append(pl.BlockSpec((q, cols), lambda i, s: (2 * s[0] + i, 0)))
        out_specs.append(pl.BlockSpec((1, q, cols), lambda i, s: (s[1], i, 0)))
        out_shape.append(jax.ShapeDtypeStruct((8, 2 * q, cols), BF16))
    return pl.pallas_call(
        body, name="cast_place",
        grid_spec=pltpu.PrefetchScalarGridSpec(num_scalar_prefetch=1, grid=(2,), in_specs=in_specs,
                                               out_specs=out_specs),
        out_shape=_out_hbm(out_shape),
        compiler_params=pltpu.CompilerParams(vmem_limit_bytes=48 << 20))(jnp.stack([c, dev]), *_in_hbm(ws))


def _gather_phases(out_refs, send_sems, recv_sems, rows=None):
    n = len(out_refs)
    x, y, c = _place()
    me, sibling = (x, y, c), (x, y, 1 - c)
    chips = [(1 - x, y), (x, 1 - y), (1 - x, 1 - y)]

    def copy(u, k, block, to):
        px, py, pc = block
        ref = out_refs[u].at[4 * px + 2 * py + pc]
        if rows is not None and rows[u] is not None:
            ref = ref.at[pl.ds(rows[u][0], rows[u][1])]
        return pltpu.make_async_remote_copy(src_ref=ref, dst_ref=ref, send_sem=send_sems.at[u, k],
                                            recv_sem=recv_sems.at[u, k], device_id=to, device_id_type=MESH)

    def start():
        for u in range(n):
            copy(u, 0, me, sibling).start()
            for j, chip in enumerate(chips):
                copy(u, 1 + j, me, (*chip, c)).start()

    def mid():
        for j, chip in enumerate(chips):
            for u in range(n):
                copy(u, 1 + j, (*chip, c), me).wait_recv()
                copy(u, 4 + j, (*chip, c), sibling).start()

    def end():
        for u in range(n):
            copy(u, 0, sibling, me).wait_recv()
        for j, chip in enumerate(chips):
            for u in range(n):
                copy(u, 4 + j, (*chip, 1 - c), me).wait_recv()
        for u in range(n):
            copy(u, 0, me, sibling).wait_send()
            for j, chip in enumerate(chips):
                copy(u, 1 + j, me, (*chip, c)).wait_send()
                copy(u, 4 + j, (*chip, c), sibling).wait_send()

    return start, mid, end


def _carry_gather(bufs, rows=None):
    n = len(bufs)
    return _Carry(bufs, [jax.ShapeDtypeStruct(b.shape, b.dtype) for b in bufs], {u: u for u in range(n)},
                  [pltpu.SemaphoreType.DMA((n, 7)), pltpu.SemaphoreType.DMA((n, 7))],
                  lambda ins, outs, sems: _gather_phases(outs, *sems, rows=rows))


def _allgather_inplace(bufs, *, name):
    n = len(bufs)

    def body(*refs):
        for phase in _gather_phases(refs[n:2 * n], *refs[2 * n:]):
            phase()

    return pl.pallas_call(
        body, name=name, out_shape=[jax.ShapeDtypeStruct(b.shape, b.dtype) for b in bufs],
        in_specs=[ANY] * n, out_specs=[ANY] * n, input_output_aliases={u: u for u in range(n)},
        scratch_shapes=[pltpu.SemaphoreType.DMA((n, 7)), pltpu.SemaphoreType.DMA((n, 7))])(*bufs)


def _ag_small(raw, sq):
    def body(dss1, dg1, dss2, dg2, dnw1, dnw2, dhw4, dqnw8, dknw2, dsk, dlg0, dlg1, sq_ref,
             out_ref, tot_ref, blk, send_sems, recv_sems, local_sems):
        blk[...] = jnp.zeros_like(blk)
        blk[0:2, :] = dss1[1]
        blk[2:3, :] = dg1[...]
        blk[3:5, :] = dss2[...]
        blk[5:6, :] = dg2[...]
        blk[6:8, :] = dss1[0]
        blk[8:9, :] = dnw1[...]
        blk[9:10, :] = dnw2[...]
        blk[10:11, 0:HGW] = dhw4[...]
        blk[10:11, HGW:D] = dqnw8[...]
        blk[11:12, 0:128] = dknw2[...]
        blk[12:14, 0:HGW] = dlg0[0]
        blk[14:16, 0:HGW] = dlg1[0]
        blk[16:24, 0:128] = dsk[...]
        blk[24:25, :] = sq_ref[...]
        _gather_blocks([blk], [out_ref], send_sems, recv_sems, local_sems)
        acc = out_ref[0]
        for i in range(1, 8):
            acc = acc + out_ref[i]
        tot_ref[...] = acc

    vm = pl.BlockSpec(memory_space=pltpu.VMEM)
    return pl.pallas_call(
        body, name="ag_small",
        out_shape=[jax.ShapeDtypeStruct((8, 32, D), F32), jax.ShapeDtypeStruct((32, D), F32)],
        in_specs=[vm] * 13, out_specs=[vm, vm],
        scratch_shapes=[pltpu.VMEM((32, D), F32)] + _gather_sems(1))(*raw, sq)


def _rs_pair_exchange(units):
    n = len(units)

    def body(*refs):
        start, _, end = _pairx_phases(refs[:n], refs[n:2 * n], *refs[2 * n:])
        start()
        end()

    return pl.pallas_call(
        body, name="rs_pair_exchange", out_shape=_pairx_shapes(units),
        in_specs=[ANY] * n, out_specs=[ANY] * n,
        scratch_shapes=[pltpu.SemaphoreType.DMA((n, 4)), pltpu.SemaphoreType.DMA((n, 4))])(*units)


def _pairx_shapes(units):
    return [jax.ShapeDtypeStruct((4,) + g.shape[2:], g.dtype) for g in units]


def _pairx_phases(g_refs, r_refs, send_sems, recv_sems):
    n = len(g_refs)
    x, y, c = _place()
    cps = [pltpu.make_async_remote_copy(
        src_ref=g_refs[u].at[j, 1 - c], dst_ref=r_refs[u].at[j], send_sem=send_sems.at[u, j],
        recv_sem=recv_sems.at[u, j], device_id=(x, y, 1 - c), device_id_type=MESH)
        for u in range(n) for j in range(4)]

    def start():
        for cp in cps:
            cp.start()

    def end():
        for cp in cps:
            cp.wait()

    return start, None, end


def _carry_pairx(units):
    n = len(units)
    return _Carry(units, _pairx_shapes(units), {},
                  [pltpu.SemaphoreType.DMA((n, 4)), pltpu.SemaphoreType.DMA((n, 4))],
                  lambda ins, outs, sems: _pairx_phases(ins, outs, *sems))


def _rs_pair_add(units, recvs, c):
    n = len(units)

    def body(c_ref, *refs):
        for u in range(n):
            refs[2 * n + u][...] = (refs[u][0].astype(F32) + refs[n + u][...].astype(F32)).astype(BF16)

    in_specs, out_specs, out_shape = [], [], []
    for g in units:
        h, w = g.shape[2] // 2, g.shape[3]
        in_specs.append(pl.BlockSpec((1, 1, h, w), lambda j, i, cr: (j, cr[0], i, 0)))
    for g in units:
        h, w = g.shape[2] // 2, g.shape[3]
        in_specs.append(pl.BlockSpec((1, h, w), lambda j, i, cr: (j, i, 0)))
        out_specs.append(pl.BlockSpec((1, h, w), lambda j, i, cr: (j, i, 0)))
        out_shape.append(jax.ShapeDtypeStruct((4, 2 * h, w), BF16))
    return pl.pallas_call(
        body, name="rs_pair_add",
        grid_spec=pltpu.PrefetchScalarGridSpec(num_scalar_prefetch=1, grid=(4, 2), in_specs=in_specs,
                                               out_specs=out_specs),
        out_shape=_out_hbm(out_shape),
        compiler_params=pltpu.CompilerParams(vmem_limit_bytes=48 << 20))(
            c.reshape(1), *_in_hbm(list(units) + list(recvs)))


def _rs_chip_exchange(pairs):
    n = len(pairs)

    def body(*refs):
        start, _, end = _chipx_phases(refs[:n], refs[n:2 * n], *refs[2 * n:])
        start()
        end()

    return pl.pallas_call(
        body, name="rs_chip_exchange", out_shape=[jax.ShapeDtypeStruct(p.shape, p.dtype) for p in pairs],
        in_specs=[ANY] * n, out_specs=[ANY] * n,
        scratch_shapes=[pltpu.SemaphoreType.DMA((n, 3)), pltpu.SemaphoreType.DMA((n, 3))])(*pairs)


def _chipx_phases(p_refs, r_refs, send_sems, recv_sems, rows=None):
    n = len(p_refs)
    x, y, c = _place()
    k = 2 * x + y

    def part(ref):
        return ref if rows is None else ref.at[pl.ds(rows[0], rows[1])]

    sends = []
    for d in range(1, 4):
        j = (k + d) % 4
        for u in range(n):
            sends.append(pltpu.make_async_remote_copy(
                src_ref=part(p_refs[u].at[j]), dst_ref=part(r_refs[u].at[k]), send_sem=send_sems.at[u, d - 1],
                recv_sem=recv_sems.at[u, d - 1], device_id=(j // 2, j % 2, c), device_id_type=MESH))

    def start():
        for cp in sends:
            cp.start()

    def end():
        for d in range(1, 4):
            src = (k + 4 - d) % 4
            for u in range(n):
                pltpu.make_async_remote_copy(
                    src_ref=part(p_refs[u].at[src]), dst_ref=part(r_refs[u].at[src]),
                    send_sem=send_sems.at[u, d - 1], recv_sem=recv_sems.at[u, d - 1], device_id=(x, y, c),
                    device_id_type=MESH).wait_recv()
        for cp in sends:
            cp.wait_send()

    return start, None, end


def _carry_chipx(pairs, rows=None, into=None):
    n = len(pairs)
    sems = [pltpu.SemaphoreType.DMA((n, 3)), pltpu.SemaphoreType.DMA((n, 3))]
    shapes = [jax.ShapeDtypeStruct(p.shape, p.dtype) for p in pairs]
    if into is None:
        return _Carry(pairs, shapes, {}, sems, lambda ins, outs, s: _chipx_phases(ins, outs, *s, rows=rows))
    return _Carry(list(pairs) + list(into), shapes, {n + u: u for u in range(n)}, sems,
                  lambda ins, outs, s: _chipx_phases(ins[:n], outs, *s, rows=rows))


def _rs_chip_add(pairs, contribs, c, chip):
    n = len(pairs)

    def body(s_ref, *refs):
        for u in range(n):
            a, b, c_, d = refs[4 * u:4 * u + 4]
            refs[4 * n + u][0] = ((a[0].astype(F32) + b[0].astype(F32)) + c_[0].astype(F32)) + d[0].astype(F32)

    in_specs, out_specs, out_shape, args = [], [], [], []
    for p, r in zip(pairs, contribs):
        h, w = p.shape[1] // 2, p.shape[2]
        in_specs += [pl.BlockSpec((1, h, w), functools.partial(lambda d, i, s: ((s[1] + d) % 4, i, 0), d))
                     for d in range(4)]
        args += [p, r, r, r]
        out_specs.append(pl.BlockSpec((1, h, w), lambda i, s: (s[0], i, 0)))
        out_shape.append(jax.ShapeDtypeStruct((2, 2 * h, w), F32))
    return pl.pallas_call(
        body, name="rs_chip_add",
        grid_spec=pltpu.PrefetchScalarGridSpec(num_scalar_prefetch=1, grid=(2,), in_specs=in_specs,
                                               out_specs=out_specs),
        out_shape=_out_hbm(out_shape),
        compiler_params=pltpu.CompilerParams(vmem_limit_bytes=48 << 20))(jnp.stack([c, chip]), *_in_hbm(args))


def _rs_sibling_gather(reds):
    n = len(reds)

    def body(*refs):
        start, _, end = _sibx_phases(refs[n:2 * n], *refs[2 * n:])
        start()
        end()

    return pl.pallas_call(
        body, name="rs_sibling_gather", out_shape=[jax.ShapeDtypeStruct(r.shape, r.dtype) for r in reds],
        in_specs=[ANY] * n, out_specs=[ANY] * n, input_output_aliases={u: u for u in range(n)},
        scratch_shapes=[pltpu.SemaphoreType.DMA((n,))] * 2)(*reds)


def _sibx_phases(o_refs, send_sems, recv_sems):
    n = len(o_refs)
    x, y, c = _place()
    cps = [pltpu.make_async_remote_copy(
        src_ref=o_refs[u].at[c], dst_ref=o_refs[u].at[c], send_sem=send_sems.at[u], recv_sem=recv_sems.at[u],
        device_id=(x, y, 1 - c), device_id_type=MESH) for u in range(n)]

    def start():
        for cp in cps:
            cp.start()

    def end():
        for u in range(n):
            cps[u].wait_send()
            pltpu.make_async_remote_copy(
                src_ref=o_refs[u].at[1 - c], dst_ref=o_refs[u].at[1 - c], send_sem=send_sems.at[u],
                recv_sem=recv_sems.at[u], device_id=(x, y, 1 - c), device_id_type=MESH).wait_recv()

    return start, None, end


def _carry_sibx(reds):
    n = len(reds)
    return _Carry(reds, [jax.ShapeDtypeStruct(r.shape, r.dtype) for r in reds], {u: u for u in range(n)},
                  [pltpu.SemaphoreType.DMA((n,))] * 2, lambda ins, outs, sems: _sibx_phases(outs, *sems))


def _prologue(blk, c_ctx, w, b, in8):
    n = w.shape[1]

    def body(blk_ref, cctx_ref, w_ref, b_ref, _in_in, g0_ref, c16_ref, g1_ref, in_ref, mod_s,
             s1, r1, l1, s2, r2, l2, s3, r3):
        start, mid, end = _gather_phases([in_ref], s3, r3)
        _gather_blocks([blk_ref], [g0_ref], s1, r1, l1)
        start()
        c16 = jnp.concatenate([g0_ref[i, 0:1, :] for i in range(8)] + [cctx_ref[...], jnp.zeros((7, D), F32)],
                              axis=0)
        c16_ref[...] = c16
        mod_s[...] = _dot(c16 * _sig(c16), w_ref[...], prec=HI) + b_ref[...]
        _gather_blocks([mod_s], [g1_ref], s2, r2, l2)
        mid()
        end()

    vm = pl.BlockSpec(memory_space=pltpu.VMEM)
    return pl.pallas_call(
        body, name="prologue",
        out_shape=[jax.ShapeDtypeStruct((8, 8, D), F32), jax.ShapeDtypeStruct((16, D), F32),
                   jax.ShapeDtypeStruct((8, 16, n), F32), jax.ShapeDtypeStruct(in8.shape, in8.dtype)],
        in_specs=[vm, vm, vm, vm, ANY], out_specs=[vm, vm, vm, ANY], input_output_aliases={4: 3},
        scratch_shapes=[pltpu.VMEM((16, n), F32)] + _gather_sems(1) + _gather_sems(1)
        + [pltpu.SemaphoreType.DMA((1, 7)), pltpu.SemaphoreType.DMA((1, 7))],
        compiler_params=pltpu.CompilerParams(vmem_limit_bytes=48 << 20))(blk, c_ctx, w, b, in8)


def _ada_bwd(c16, dmod16, w, carry=None):
    n = w.shape[1]
    tn = 512

    def body(c_ref, d_ref, w_ref, gw_ref, gc_ref):
        j = pl.program_id(0)

        @pl.when(j == 0)
        def _():
            gc_ref[...] = jnp.zeros_like(gc_ref)

        cc = c_ref[...]
        dm = d_ref[...]
        gw_ref[...] = _dot(cc * _sig(cc), dm, TN, prec=HI)
        gc_ref[...] += _dot(dm, w_ref[...], NT, prec=HI)

    return _pcall(body, name="ada_bwd", grid=(n // tn,),
                  in_specs=[_full((16, D)), pl.BlockSpec((16, tn), lambda j: (0, j)),
                            pl.BlockSpec((D, tn), lambda j: (0, j))],
                  out_specs=[pl.BlockSpec((D, tn), lambda j: (0, j)), _full((16, D))],
                  out_shape=[jax.ShapeDtypeStruct((D, n), F32),
                             jax.ShapeDtypeStruct((16, D), F32)], carry=carry)(c16, dmod16, w)


def _adam_math(w, g, m, v):
    c1 = 1.0 - ADAM_B1 ** ADAM_STEP
    c2 = 1.0 - ADAM_B2 ** ADAM_STEP
    nm = ADAM_B1 * m + (1.0 - ADAM_B1) * g
    nv = ADAM_B2 * v + (1.0 - ADAM_B2) * (g * g)
    return -ADAM_LR * ((nm / c1) / (jnp.sqrt(nv / c2) + ADAM_EPS) + ADAM_WD * w), nm, nv


def _adamw_small(ws, gs, ms, vs):
    n = len(ws)

    def body(*refs):
        for u in range(n):
            d_, nm, nv = _adam_math(refs[u][...], refs[n + u][...], refs[2 * n + u][...], refs[3 * n + u][...])
            refs[4 * n + u][...] = d_
            refs[5 * n + u][...] = nm
            refs[6 * n + u][...] = nv

    specs = [_full(w.shape) for w in ws]
    shapes = [jax.ShapeDtypeStruct(w.shape, F32) for w in ws]
    out = _pcall(body, name="adamw_small", grid=(1,), in_specs=specs * 4, out_specs=specs * 3,
                 out_shape=shapes * 3)(*ws, *gs, *ms, *vs)
    return out[:n], out[n:2 * n], out[2 * n:]


def _cctx_grad(parts, c_ctx):
    def body(p_ref, c_ref, o_ref):
        acc = p_ref[0:1, :]
        for k in range(1, 4):
            acc = acc + p_ref[k:k + 1, :]
        cc = c_ref[...]
        s = _sig(cc)
        o_ref[...] = acc * (s * (1.0 + cc * (1.0 - s)))

    return _pcall(body, name="cctx_grad", grid=(1,), in_specs=[_full(parts.shape), _full((1, D))],
                  out_specs=_full((1, D)), out_shape=jax.ShapeDtypeStruct((1, D), F32))(parts, c_ctx)


ADAM_STEPS = 8


def _adamw_multi(ws, gs, ms, vs, *, name, carry=None):
    n = len(ws)

    def body(*refs):
        for u in range(n):
            refs[4 * n + u][...], refs[5 * n + u][...], refs[6 * n + u][...] = _adam_math(
                refs[u][...], refs[n + u][...], refs[2 * n + u][...], refs[3 * n + u][...])

    specs = [pl.BlockSpec((w.shape[0] // ADAM_STEPS, w.shape[1]), lambda i: (i, 0)) for w in ws]
    shapes = [jax.ShapeDtypeStruct(w.shape, F32) for w in ws]
    res = _pcall(body, name=name, grid=(ADAM_STEPS,), in_specs=specs * 4, out_specs=specs * 3,
                 out_shape=shapes * 3, carry=carry)(*ws, *gs, *ms, *vs)
    out, extra = res if carry is not None else (res, None)
    return (out[:n], out[n:2 * n], out[2 * n:]), extra


def kernel(x, c, ctx, c_ctx, w_ada, b_ada, norm_mix_w, norm_ffn_w, w_in, hgrn_lb_logits, hgrn_norm_w, q_norm_w, k_norm_w, attn_sinks, w_branch_hgrn, w_branch_attn, w_out, w_ffn_gate, w_ffn_up, w_ffn_down, loss_target, m_c_ctx, m_w_ada, m_b_ada, m_norm_mix_w, m_norm_ffn_w, m_w_in, m_hgrn_lb_logits, m_hgrn_norm_w, m_q_norm_w, m_k_norm_w, m_attn_sinks, m_w_branch_hgrn, m_w_branch_attn, m_w_out, m_w_ffn_gate, m_w_ffn_up, m_w_ffn_down, v_c_ctx, v_w_ada, v_b_ada, v_norm_mix_w, v_norm_ffn_w, v_w_in, v_hgrn_lb_logits, v_hgrn_norm_w, v_q_norm_w, v_k_norm_w, v_attn_sinks, v_w_branch_hgrn, v_w_branch_attn, v_w_out, v_w_ffn_gate, v_w_ffn_up, v_w_ffn_down):
    xi, yi, ci = _place()
    chip = 2 * xi + yi
    dev = 2 * chip + ci
    s_len = x.shape[1]

    shards = [w_in[0].T, w_branch_hgrn[0], w_branch_attn[0], w_out[0], w_ffn_gate[0].T, w_ffn_up[0].T,
              w_ffn_down[0]]
    bufs = _cast_place(shards, ci, dev)

    lbrow = jnp.pad(hgrn_lb_logits.reshape(1, 512), ((0, 0), (0, D - 512)))
    blk = jnp.concatenate([c, lbrow, jnp.zeros((6, D), F32)], axis=0)
    nada = w_ada.shape[2]
    b_sh = lax.dynamic_slice(b_ada, (0, chip * nada), (1, nada))
    g0, c16, g1, in8 = _prologue(blk, c_ctx[None], w_ada[0], b_sh, bufs[0])
    lg = g0[0::2, 1, :512].reshape(4, 2, 2, 128).transpose(1, 2, 0, 3).reshape(2, 2, HGW)
    modall = g1[0::2].transpose(1, 0, 2).reshape(16, 4 * nada)
    mod = lax.dynamic_slice(modall, (dev, 0), (1, 6 * D)).reshape(6, D)
    modc = modall[8].reshape(6, D)[:2]

    sq, gx, _, small, rs = _local_step(
        x[0], ctx[0], loss_target[0], mod, modc, norm_mix_w, norm_ffn_w, lg, hgrn_norm_w, q_norm_w,
        k_norm_w, attn_sinks[0], in8.reshape(NCOL, D), bufs[1:], dist=(ci, chip))

    def whole(r):
        return r.reshape(2 * r.shape[1], r.shape[2])

    g_dn, g_g, g_u = [whole(r) for r in rs["ffn_done"]]
    g_bh, g_ba, g_o = [whole(r) for r in rs["mix_done"]]
    in_pairs = rs["in_pairs"]
    rest_rows = (in_pairs[0].shape[1] // 2, in_pairs[0].shape[1] // 2)

    g2, tot = _ag_small(small["raw"], sq)
    loss = 0.5 * jnp.sum(tot[24]) / D
    dmodc_tot = jnp.pad(tot[6:8].reshape(1, 2 * D), ((0, 0), (0, 4 * D)))
    g_b_ada = tot[0:6].reshape(1, 6 * D) + dmodc_tot
    dmod16 = jnp.concatenate([g2[:, 0:6].reshape(8, 6 * D), dmodc_tot, jnp.zeros((7, 6 * D), F32)], axis=0)
    (g_w_ada, gc_part), in_contribs = _ada_bwd(
        c16, lax.dynamic_slice(dmod16, (0, chip * nada), (16, nada)), w_ada[0],
        carry=_carry_chipx(in_pairs, rows=rest_rows, into=rs["in_part"]))
    g3, = _allgather([gc_part[8:16]], name="ag_cctx", in_vmem=True)
    g_c_ctx = _cctx_grad(g3[0::2, 0], c_ctx[None])[0]
    g_nw1 = tot[8:9]
    g_nw2 = tot[9:10]
    g_hw = tot[10, :HGW].reshape(4, HGD).sum(0, keepdims=True)
    g_qnw = tot[10, HGW:].reshape(8, HDIM).sum(0, keepdims=True)
    g_knw = tot[11, :128].reshape(2, HDIM).sum(0, keepdims=True)
    g_sinks = tot[16:24, 0][None]
    g_lg = lax.dynamic_slice(tot[12:16, :HGW].reshape(2, 2, HGW), (0, 0, chip * 128), (2, 2, 128))

    names = ["c_ctx", "w_ada", "b_ada", "norm_mix_w", "norm_ffn_w", "w_in", "hgrn_lb_logits", "hgrn_norm_w",
             "q_norm_w", "k_norm_w", "attn_sinks", "w_branch_hgrn", "w_branch_attn", "w_out", "w_ffn_gate",
             "w_ffn_up", "w_ffn_down"]
    ws = dict(zip(names, [c_ctx, w_ada, b_ada, norm_mix_w, norm_ffn_w, w_in, hgrn_lb_logits, hgrn_norm_w,
                          q_norm_w, k_norm_w, attn_sinks, w_branch_hgrn, w_branch_attn, w_out, w_ffn_gate,
                          w_ffn_up, w_ffn_down]))
    ms = dict(zip(names, [m_c_ctx, m_w_ada, m_b_ada, m_norm_mix_w, m_norm_ffn_w, m_w_in, m_hgrn_lb_logits,
                          m_hgrn_norm_w, m_q_norm_w, m_k_norm_w, m_attn_sinks, m_w_branch_hgrn,
                          m_w_branch_attn, m_w_out, m_w_ffn_gate, m_w_ffn_up, m_w_ffn_down]))
    vs = dict(zip(names, [v_c_ctx, v_w_ada, v_b_ada, v_norm_mix_w, v_norm_ffn_w, v_w_in, v_hgrn_lb_logits,
                          v_hgrn_norm_w, v_q_norm_w, v_k_norm_w, v_attn_sinks, v_w_branch_hgrn,
                          v_w_branch_attn, v_w_out, v_w_ffn_gate, v_w_ffn_up, v_w_ffn_down]))
    transposed = ("w_in", "w_ffn_gate", "w_ffn_up")

    def view(a, n):
        return a[0].T if n in transposed else a[0]

    def unview(a, n):
        return a.T[None] if n in transposed else a[None]

    delta, new_m, new_v, grads = {}, {}, {}, {}

    def big_adamw(group, gs, name, carry=None):
        (d_, m_, v_), extra = _adamw_multi([view(ws[n], n) for n in group], gs, [view(ms[n], n) for n in group],
                                           [view(vs[n], n) for n in group], name=name, carry=carry)
        for i, n in enumerate(group):
            grads[n], delta[n], new_m[n], new_v[n] = (unview(gs[i], n), unview(d_[i], n), unview(m_[i], n),
                                                      unview(v_[i], n))
        return extra

    big_adamw(["w_ffn_down", "w_ffn_gate", "w_ffn_up", "w_out", "w_branch_hgrn", "w_branch_attn"],
              [g_dn, g_g, g_u, g_o, g_bh, g_ba], "adamw_first")
    in_reds = _rs_chip_add(in_pairs, in_contribs, ci, chip)
    g_in, = [whole(r) for r in _rs_sibling_gather(in_reds)]
    big_adamw(["w_in", "w_ada"], [g_in, g_w_ada], "adamw_second")
    grads.update(c_ctx=g_c_ctx, b_ada=g_b_ada, norm_mix_w=g_nw1, norm_ffn_w=g_nw2, hgrn_lb_logits=g_lg,
                 hgrn_norm_w=g_hw, q_norm_w=g_qnw, k_norm_w=g_knw, attn_sinks=g_sinks)
    small_names = [n for n in names if n not in delta]

    def two_d(a):
        return a.reshape(1, -1) if a.ndim == 1 else a

    sd, sm_, sv = _adamw_small(*[[two_d(d[n]) for n in small_names] for d in (ws, grads, ms, vs)])
    for i, n in enumerate(small_names):
        for dst, src in ((delta, sd), (new_m, sm_), (new_v, sv)):
            dst[n] = src[i].reshape(ws[n].shape)
    return (loss, gx[None], *[grads[n] for n in names], *[delta[n] for n in names],
            *[new_m[n] for n in names], *[new_v[n] for n in names])
```

```python
import functools

import numpy as np
import jax
import jax.numpy as jnp
from jax import lax
from jax.experimental import pallas as pl
from jax.experimental.pallas import tpu as pltpu

F32 = jnp.float32
BF16 = jnp.bfloat16
HI = lax.Precision.HIGHEST
MESH = pl.DeviceIdType.MESH

D = 1024
L = 256
TM = 256
HGW = 512
HGD = 128
CH = 32
ATW = 512
HDIM = 64
BLK = 128
GRID_W = 64
DFF = 2816
NCOL = 5376
EPS = 1e-6
ROPE_THETA = 10000.0

C_FB, C_INP, C_QHG, C_FF = 0, 1, 2, 3
C_GATES = 1
C_GHG, C_QRAW = 8, 9
C_KV = 20
C_QKV = 6

ADAM_LR, ADAM_B1, ADAM_B2, ADAM_EPS, ADAM_WD, ADAM_STEP = 0.001, 0.9, 0.999, 1e-08, 0.01, 10

NN = (((1,), (0,)), ((), ()))
NT = (((1,), (1,)), ((), ()))
TN = (((0,), (0,)), ((), ()))


def _dot(a, b, dims=NN, prec=None):
    return lax.dot_general(a, b, dims, precision=prec, preferred_element_type=F32)


def _bdot(a, b, dims=NN):
    return _dot(a.astype(BF16), b.astype(BF16), dims)


def _sig(x):
    return 1.0 / (1.0 + jnp.exp(-x))


class _Carry:
    def __init__(self, ins, outs, aliases, scratch, phases, peers):
        self.ins, self.outs, self.aliases, self.scratch, self.phases = ins, outs, aliases, scratch, phases
        self.peers = peers


BARRIER_IDS = {"sib": 1, "chips": 2, "both": 3}


def _peer_barrier(kind):
    x, y, c = _place()
    peers = []
    if kind in ("sib", "both"):
        peers.append((x, y, 1 - c))
    if kind in ("chips", "both"):
        peers += [(1 - x, y, c), (x, 1 - y, c), (1 - x, 1 - y, c)]
    bar = pltpu.get_barrier_semaphore()
    for peer in peers:
        pl.semaphore_signal(bar, inc=1, device_id=peer, device_id_type=MESH)
    pl.semaphore_wait(bar, len(peers))


def _in_hbm(args):
    return [pltpu.with_memory_space_constraint(a, pltpu.HBM) for a in args]


def _out_hbm(shapes):
    if isinstance(shapes, (list, tuple)):
        return [pltpu.HBM(s.shape, s.dtype) for s in shapes]
    return pltpu.HBM(shapes.shape, shapes.dtype)


def _carry_join(a, b):
    na_in, na_out, na_sc = len(a.ins), len(a.outs), len(a.scratch)
    aliases = dict(a.aliases)
    aliases.update({na_in + i: na_out + o for i, o in b.aliases.items()})

    def phases(ins, outs, sems):
        pa = a.phases(ins[:na_in], outs[:na_out], sems[:na_sc])
        pb = b.phases(ins[na_in:], outs[na_out:], sems[na_sc:])

        def both(fa, fb):
            if fa is None and fb is None:
                return None

            def run():
                for fn in (fa, fb):
                    if fn is not None:
                        fn()
            return run

        return tuple(both(fa, fb) for fa, fb in zip(pa, pb))

    return _Carry(list(a.ins) + list(b.ins), list(a.outs) + list(b.outs), aliases,
                  list(a.scratch) + list(b.scratch), phases, a.peers if a.peers == b.peers else "both")


def _pcall(body, *, name, grid, in_specs, out_specs, out_shape, scratch=(), aliases=None, vmem_mb=48,
           carry=None):
    params = pltpu.CompilerParams(dimension_semantics=("arbitrary",) * len(grid),
                                  vmem_limit_bytes=vmem_mb << 20)
    if carry is None:
        plain = pl.pallas_call(
            body, name=name, grid=grid, in_specs=in_specs, out_specs=out_specs, out_shape=_out_hbm(out_shape),
            scratch_shapes=list(scratch), input_output_aliases=aliases or {}, compiler_params=params)
        return lambda *args: plain(*_in_hbm(args))
    single = not isinstance(out_shape, (list, tuple))
    out_specs_l = [out_specs] if single else list(out_specs)
    out_shape_l = [out_shape] if single else list(out_shape)
    n_in, n_out, n_sc = len(in_specs), len(out_shape_l), len(scratch)
    k_in, k_out = len(carry.ins), len(carry.outs)
    nsteps = int(np.prod(grid))
    assert nsteps >= 3

    def wrapped(*refs):
        ins, cins = refs[:n_in], refs[n_in:n_in + k_in]
        o0 = n_in + k_in
        outs, couts = refs[o0:o0 + n_out], refs[o0 + n_out:o0 + n_out + k_out]
        s0 = o0 + n_out + k_out
        sc, csc = refs[s0:s0 + n_sc], refs[s0 + n_sc:]
        step = pl.program_id(0)
        for ax in range(1, len(grid)):
            step = step * grid[ax] + pl.program_id(ax)
        start, mid, end = carry.phases(cins, couts, csc)

        @pl.when(step == 0)
        def _():
            _peer_barrier(carry.peers)
            start()

        body(*ins, *outs, *sc)
        if mid is not None:
            pl.when(step == nsteps - 2)(mid)
        pl.when(step == nsteps - 1)(end)

    all_aliases = dict(aliases or {})
    all_aliases.update({n_in + i: n_out + o for i, o in carry.aliases.items()})
    call = pl.pallas_call(
        wrapped, name=name, grid=grid, in_specs=list(in_specs) + [ANY] * k_in,
        out_specs=out_specs_l + [ANY] * k_out, out_shape=_out_hbm(out_shape_l + list(carry.outs)),
        scratch_shapes=list(scratch) + list(carry.scratch), input_output_aliases=all_aliases,
        compiler_params=pltpu.CompilerParams(dimension_semantics=("arbitrary",) * len(grid),
                                             vmem_limit_bytes=vmem_mb << 20,
                                             collective_id=BARRIER_IDS[carry.peers]))

    def run(*args):
        res = call(*_in_hbm(args), *carry.ins)
        core = res[:n_out]
        return (core[0] if single else list(core)), list(res[n_out:])

    return run


def _full(shape):
    nd = len(shape)
    return pl.BlockSpec(shape, lambda *_: (0,) * nd)


ANY = pl.BlockSpec(memory_space=pl.ANY)


def _mm(a, b, *, name, mode="nn", out_dtype=F32, tm, tn, tk):
    if mode == "nn":
        (m, k), (k2, n) = a.shape, b.shape
    elif mode == "nt":
        (m, k), (n, k2) = a.shape, b.shape
    else:
        (k, m), (k2, n) = a.shape, b.shape
    assert k == k2 and m % tm == 0 and n % tn == 0 and k % tk == 0, (name, a.shape, b.shape)
    nk = k // tk
    dims = {"nn": NN, "nt": NT, "tn": TN}[mode]

    def body(a_ref, b_ref, o_ref, acc):
        kk = pl.program_id(2)

        @pl.when(kk == 0)
        def _():
            acc[...] = jnp.zeros_like(acc)

        acc[...] += _bdot(a_ref[...], b_ref[...], dims)

        @pl.when(kk == nk - 1)
        def _():
            o_ref[...] = acc[...].astype(out_dtype)

    a_spec = (pl.BlockSpec((tk, tm), lambda i, j, kk: (kk, i)) if mode == "tn"
              else pl.BlockSpec((tm, tk), lambda i, j, kk: (i, kk)))
    b_spec = (pl.BlockSpec((tn, tk), lambda i, j, kk: (j, kk)) if mode == "nt"
              else pl.BlockSpec((tk, tn), lambda i, j, kk: (kk, j)))
    return _pcall(body, name=name, grid=(m // tm, n // tn, nk), in_specs=[a_spec, b_spec],
                  out_specs=pl.BlockSpec((tm, tn), lambda i, j, kk: (i, j)),
                  out_shape=jax.ShapeDtypeStruct((m, n), out_dtype),
                  scratch=[pltpu.VMEM((tm, tn), F32)])(a, b)


NT_IN = NCOL // 256


def _src_block(j):
    return j + jnp.where(j < 4, 2, jnp.where(j < 6, 3, jnp.where(j < 8, -6, jnp.where(
        j < 16, 5, jnp.where(j < 20, -7, -14)))))


def _mm_in(h, wt, tm, carry=None):
    tt = h.shape[0]

    def body(h_ref, w_ref, o_ref):
        o_ref[...] = _bdot(h_ref[...], w_ref[...], NT)

    return _pcall(body, name="mm_in", grid=(tt // tm, NT_IN),
                  in_specs=[pl.BlockSpec((tm, D), lambda i, j: (i, 0)),
                            pl.BlockSpec((256, D), lambda i, j: (_src_block(j), 0))],
                  out_specs=pl.BlockSpec((tm, 256), lambda i, j: (i, j)),
                  out_shape=jax.ShapeDtypeStruct((tt, NCOL), F32), carry=carry)(h, wt)


def _mm_dh(dp, wt, tm, carry=None):
    tt = dp.shape[0]
    per, ng = 3, NT_IN // 3

    def body(d_ref, w0, w1, w2, o_ref, acc):
        kk = pl.program_id(1)

        @pl.when(kk == 0)
        def _():
            acc[...] = jnp.zeros_like(acc)

        acc[...] += (_bdot(d_ref[:, 0:256], w0[...]) + _bdot(d_ref[:, 256:512], w1[...])
                     + _bdot(d_ref[:, 512:768], w2[...]))

        @pl.when(kk == ng - 1)
        def _():
            o_ref[...] = acc[...]

    wspecs = [pl.BlockSpec((256, D), functools.partial(lambda t, i, kk: (_src_block(per * kk + t), 0), t))
              for t in range(per)]
    return _pcall(body, name="mm_dh", grid=(tt // tm, ng),
                  in_specs=[pl.BlockSpec((tm, per * 256), lambda i, kk: (i, kk))] + wspecs,
                  out_specs=pl.BlockSpec((tm, D), lambda i, kk: (i, 0)),
                  out_shape=jax.ShapeDtypeStruct((tt, D), F32), scratch=[pltpu.VMEM((tm, D), F32)],
                  carry=carry)(dp, wt, wt, wt)


def _mm_gin(dp, h, tk):
    tt = dp.shape[0]
    nk = tt // tk

    def body(d_ref, h_ref, o_ref, acc):
        kk = pl.program_id(1)

        @pl.when(kk == 0)
        def _():
            acc[...] = jnp.zeros_like(acc)

        acc[...] += _bdot(d_ref[...], h_ref[...], TN)

        @pl.when(kk == nk - 1)
        def _():
            o_ref[...] = acc[...].astype(BF16)

    return _pcall(body, name="mm_gin", grid=(NT_IN, nk),
                  in_specs=[pl.BlockSpec((tk, 256), lambda j, kk: (kk, j)),
                            pl.BlockSpec((tk, D), lambda j, kk: (kk, 0))],
                  out_specs=pl.BlockSpec((256, D), lambda j, kk: (_src_block(j), 0)),
                  out_shape=jax.ShapeDtypeStruct((NCOL, D), BF16), scratch=[pltpu.VMEM((256, D), F32)])(dp, h)


def _tok_specs():
    assert L == TM
    return [_full((TM, D)), pl.BlockSpec((TM, D), lambda i: (jnp.maximum(i - 1, 0), 0))]


def _mod1(ctx, x, nw, ss):
    rows = L + x.shape[0]

    def body(c_ref, x_ref, nw_ref, ss_ref, h_ref):
        t = jnp.where(pl.program_id(0) == 0, c_ref[...], x_ref[...])
        r = lax.rsqrt(jnp.mean(t * t, axis=-1, keepdims=True) + EPS)
        s = ss_ref[0]
        h_ref[...] = ((t * r * nw_ref[...]) * (1.0 + s[1:2]) + s[0:1]).astype(BF16)

    return _pcall(body, name="mod1", grid=(rows // TM,),
                  in_specs=_tok_specs() + [_full((1, D)),
                                           pl.BlockSpec((1, 2, D), lambda i: (jnp.minimum(i, 1), 0, 0))],
                  out_specs=pl.BlockSpec((TM, D), lambda i: (i, 0)),
                  out_shape=jax.ShapeDtypeStruct((rows, D), BF16))(ctx, x, nw, ss)


def _norm_bwd_rows(x, dh, nw, scale):
    r = lax.rsqrt(jnp.mean(x * x, axis=-1, keepdims=True) + EPS)
    xh = x * r
    dxh = dh * ((1.0 + scale) * nw)
    dx = r * (dxh - xh * jnp.mean(dxh * xh, axis=-1, keepdims=True))
    return dx, xh


def _out_proj_mod2(mixed, w_o, x, g1, nw2, ss2):
    s_len = x.shape[0]
    tm = 512

    def body(m_ref, w_ref, x_ref, g_ref, nw_ref, ss_ref, ao_ref, x1_ref, h_ref):
        ao = _bdot(m_ref[...], w_ref[...])
        ao_ref[...] = ao.astype(BF16)
        x1 = x_ref[...] + g_ref[...] * ao
        x1_ref[...] = x1
        r = lax.rsqrt(jnp.mean(x1 * x1, axis=-1, keepdims=True) + EPS)
        s = ss_ref[0]
        h_ref[...] = ((x1 * r * nw_ref[...]) * (1.0 + s[1:2]) + s[0:1]).astype(BF16)

    row = pl.BlockSpec((tm, D), lambda i: (i, 0))
    f = jax.ShapeDtypeStruct((s_len, D), F32)
    return _pcall(body, name="out_proj_mod2", grid=(s_len // tm,),
                  in_specs=[row, _full((D, D)), row, _full((1, D)), _full((1, D)), _full((1, 2, D))],
                  out_specs=[row, row, row],
                  out_shape=[jax.ShapeDtypeStruct((s_len, D), BF16), f,
                             jax.ShapeDtypeStruct((s_len, D), BF16)])(mixed, w_o, x, g1, nw2, ss2)


TS = 1024


def _acc_call(body, *, name, grid, in_specs, out_specs, out_shape, acc_shapes, args, carry=None):
    return _pcall(body, name=name, grid=grid, in_specs=in_specs, out_specs=out_specs, out_shape=out_shape,
                  scratch=[pltpu.VMEM(s, F32) for s in acc_shapes], carry=carry)(*args)


def _mm_cs(a, w4, *, name):
    m, k = a.shape
    _, _, ns = w4.shape

    def body(a_ref, w_ref, o_ref):
        o_ref[...] = _bdot(a_ref[...], w_ref[0])

    return _pcall(body, name=name, grid=(m // TS, 4),
                  in_specs=[pl.BlockSpec((TS, k), lambda i, j: (i, 0)),
                            pl.BlockSpec((1, k, ns), lambda i, j: (j, 0, 0))],
                  out_specs=pl.BlockSpec((TS, ns), lambda i, j: (i, j)),
                  out_shape=jax.ShapeDtypeStruct((m, 4 * ns), F32))(a, w4)


def _mm_cs_nt(a, w4, *, name):
    m = a.shape[0]
    _, k, ns = w4.shape

    def body(a_ref, w_ref, o_ref, acc):
        j = pl.program_id(1)

        @pl.when(j == 0)
        def _():
            acc[...] = jnp.zeros_like(acc)

        acc[...] += _bdot(a_ref[...], w_ref[0], NT)

        @pl.when(j == 3)
        def _():
            o_ref[...] = acc[...]

    return _acc_call(body, name=name, grid=(m // TS, 4),
                     in_specs=[pl.BlockSpec((TS, ns), lambda i, j: (i, j)),
                               pl.BlockSpec((1, k, ns), lambda i, j: (j, 0, 0))],
                     out_specs=pl.BlockSpec((TS, k), lambda i, j: (i, 0)),
                     out_shape=jax.ShapeDtypeStruct((m, k), F32), acc_shapes=[(TS, k)], args=(a, w4))


def _mm_cs_tn(a, b, ns, *, name):
    s_len, k = a.shape
    nk = s_len // TS

    def body(a_ref, b_ref, o_ref, acc):
        t = pl.program_id(1)

        @pl.when(t == 0)
        def _():
            acc[...] = jnp.zeros_like(acc)

        acc[...] += _bdot(a_ref[...], b_ref[...], TN)

        @pl.when(t == nk - 1)
        def _():
            o_ref[0] = acc[...].astype(o_ref.dtype)

    return _acc_call(body, name=name, grid=(4, nk),
                     in_specs=[pl.BlockSpec((TS, k), lambda j, t: (t, 0)),
                               pl.BlockSpec((TS, ns), lambda j, t: (t, j))],
                     out_specs=pl.BlockSpec((1, k, ns), lambda j, t: (j, 0, 0)),
                     out_shape=jax.ShapeDtypeStruct((4, k, ns), BF16), acc_shapes=[(k, ns)], args=(a, b))


def _ffn_up(h2, g4, u4, carry=None):
    s_len = h2.shape[0]
    ns = g4.shape[1]

    def body(h_ref, g_ref, u_ref, a_ref, b_ref, z_ref):
        h = h_ref[...]
        a = _bdot(h, g_ref[0], NT)
        b = _bdot(h, u_ref[0], NT)
        a_ref[0] = a.astype(BF16)
        b_ref[0] = b.astype(BF16)
        z_ref[0] = (a * _sig(a) * b).astype(BF16)

    w = pl.BlockSpec((1, ns, D), lambda i, j: (j, 0, 0))
    o = pl.BlockSpec((1, TS, ns), lambda i, j: (j, i, 0))
    f = jax.ShapeDtypeStruct((4, s_len, ns), BF16)
    return _pcall(body, name="ffn_up", grid=(s_len // TS, 4),
                  in_specs=[pl.BlockSpec((TS, D), lambda i, j: (i, 0)), w, w], out_specs=[o, o, o],
                  out_shape=[f, f, jax.ShapeDtypeStruct((4, s_len, ns), BF16)], carry=carry)(h2, g4, u4)


def _ffn_down_loss(z4, dn4, x1, g2, tgt):
    _, s_len, ns = z4.shape

    def body(z_ref, w_ref, x1_ref, g_ref, t_ref, sq_ref, dx2_ref, dyb_ref, dg_ref, acc):
        i, j = pl.program_id(0), pl.program_id(1)

        @pl.when((i == 0) & (j == 0))
        def _():
            sq_ref[...] = jnp.zeros_like(sq_ref)
            dg_ref[...] = jnp.zeros_like(dg_ref)

        @pl.when(j == 0)
        def _():
            acc[...] = jnp.zeros_like(acc)

        acc[...] += _bdot(z_ref[0], w_ref[0])

        @pl.when(j == 3)
        def _():
            y_ = acc[...]
            g = g_ref[...]
            e = x1_ref[...] + g * y_ - t_ref[...]
            sq_ref[...] += jnp.sum(e * e, axis=0, keepdims=True)
            dx2 = e * (1.0 / D)
            dx2_ref[...] = dx2
            dyb_ref[...] = (g * dx2).astype(BF16)
            dg_ref[...] += jnp.sum(dx2 * y_, axis=0, keepdims=True)

    row = pl.BlockSpec((TS, D), lambda i, j: (i, 0))
    vec = _full((1, D))
    return _acc_call(body, name="ffn_down_loss", grid=(s_len // TS, 4),
                     in_specs=[pl.BlockSpec((1, TS, ns), lambda i, j: (j, i, 0)),
                               pl.BlockSpec((1, ns, D), lambda i, j: (j, 0, 0)), row, vec, row],
                     out_specs=[vec, row, row, vec],
                     out_shape=[jax.ShapeDtypeStruct((1, D), F32), jax.ShapeDtypeStruct((s_len, D), F32),
                                jax.ShapeDtypeStruct((s_len, D), BF16), jax.ShapeDtypeStruct((1, D), F32)],
                     acc_shapes=[(TS, D)], args=(z4, dn4, x1, g2, tgt))


def _ffn_dz(dyb, dn4, a4, b4):
    _, s_len, ns = a4.shape

    def body(dy_ref, w_ref, a_ref, b_ref, da_ref, db_ref):
        dz = _bdot(dy_ref[...], w_ref[0], NT)
        a = a_ref[0].astype(F32)
        s = _sig(a)
        da_ref[0] = (dz * b_ref[0].astype(F32) * (s * (1.0 + a * (1.0 - s)))).astype(BF16)
        db_ref[0] = (dz * (a * s)).astype(BF16)

    t = pl.BlockSpec((1, TS, ns), lambda i, j: (j, i, 0))
    o = jax.ShapeDtypeStruct((4, s_len, ns), BF16)
    return _pcall(body, name="ffn_dz", grid=(s_len // TS, 4),
                  in_specs=[pl.BlockSpec((TS, D), lambda i, j: (i, 0)),
                            pl.BlockSpec((1, ns, D), lambda i, j: (j, 0, 0)), t, t],
                  out_specs=[t, t], out_shape=[o, o])(dyb, dn4, a4, b4)


def _ffn_gdn(z4, dyb):
    _, s_len, ns = z4.shape
    nk = s_len // TS

    def body(z_ref, dy_ref, o_ref, acc):
        t = pl.program_id(1)

        @pl.when(t == 0)
        def _():
            acc[...] = jnp.zeros_like(acc)

        acc[...] += _bdot(z_ref[0], dy_ref[...], TN)

        @pl.when(t == nk - 1)
        def _():
            o_ref[0] = acc[...].astype(o_ref.dtype)

    return _acc_call(body, name="ffn_gdn", grid=(4, nk),
                     in_specs=[pl.BlockSpec((1, TS, ns), lambda j, t: (j, t, 0)),
                               pl.BlockSpec((TS, D), lambda j, t: (t, 0))],
                     out_specs=pl.BlockSpec((1, ns, D), lambda j, t: (j, 0, 0)),
                     out_shape=jax.ShapeDtypeStruct((4, ns, D), BF16), acc_shapes=[(ns, D)], args=(z4, dyb))


def _ffn_dh2(da4, db4, g4, u4, carry=None):
    _, s_len, ns = da4.shape

    def body(da_ref, db_ref, g_ref, u_ref, o_ref, acc):
        j = pl.program_id(1)

        @pl.when(j == 0)
        def _():
            acc[...] = jnp.zeros_like(acc)

        acc[...] += _bdot(da_ref[0], g_ref[0]) + _bdot(db_ref[0], u_ref[0])

        @pl.when(j == 3)
        def _():
            o_ref[...] = acc[...]

    t = pl.BlockSpec((1, TS, ns), lambda i, j: (j, i, 0))
    w = pl.BlockSpec((1, ns, D), lambda i, j: (j, 0, 0))
    return _acc_call(body, name="ffn_dh2", grid=(s_len // TS, 4), in_specs=[t, t, w, w],
                     out_specs=pl.BlockSpec((TS, D), lambda i, j: (i, 0)),
                     out_shape=jax.ShapeDtypeStruct((s_len, D), F32), acc_shapes=[(TS, D)],
                     args=(da4, db4, g4, u4), carry=carry)


def _ffn_ggu(h2, da4, db4, carry=None):
    _, s_len, ns = da4.shape
    nk = s_len // TS

    def body(h_ref, da_ref, db_ref, gg_ref, gu_ref, acc_g, acc_u):
        t = pl.program_id(1)

        @pl.when(t == 0)
        def _():
            acc_g[...] = jnp.zeros_like(acc_g)
            acc_u[...] = jnp.zeros_like(acc_u)

        h = h_ref[...]
        acc_g[...] += _bdot(da_ref[0], h, TN)
        acc_u[...] += _bdot(db_ref[0], h, TN)

        @pl.when(t == nk - 1)
        def _():
            gg_ref[0] = acc_g[...].astype(BF16)
            gu_ref[0] = acc_u[...].astype(BF16)

    d = pl.BlockSpec((1, TS, ns), lambda j, t: (j, t, 0))
    o = pl.BlockSpec((1, ns, D), lambda j, t: (j, 0, 0))
    f = jax.ShapeDtypeStruct((4, ns, D), BF16)
    return _acc_call(body, name="ffn_ggu", grid=(4, nk),
                     in_specs=[pl.BlockSpec((TS, D), lambda j, t: (t, 0)), d, d], out_specs=[o, o],
                     out_shape=[f, f], acc_shapes=[(ns, D), (ns, D)], args=(h2, da4, db4), carry=carry)


def _mod2_bwd(x1, dh2, dx2, ao, nw2, ss2, g1):
    s_len = x1.shape[0]

    def body(x1_ref, dh_ref, dx2_ref, ao_ref, nw_ref, ss_ref, g_ref,
             dx1_ref, da_ref, dss_ref, dnw_ref, dg_ref):
        i = pl.program_id(0)

        @pl.when(i == 0)
        def _():
            dss_ref[...] = jnp.zeros_like(dss_ref)
            dnw_ref[...] = jnp.zeros_like(dnw_ref)
            dg_ref[...] = jnp.zeros_like(dg_ref)

        dh = dh_ref[...]
        nw = nw_ref[...]
        scale = ss_ref[0][1:2]
        dxn, xh = _norm_bwd_rows(x1_ref[...], dh, nw, scale)
        dx1 = dx2_ref[...] + dxn
        dx1_ref[...] = dx1
        da_ref[...] = (g_ref[...] * dx1).astype(BF16)
        dg_ref[...] += jnp.sum(dx1 * ao_ref[...].astype(F32), axis=0, keepdims=True)
        dsh = jnp.sum(dh, axis=0, keepdims=True)
        dsc = jnp.sum(dh * xh * nw, axis=0, keepdims=True)
        dss_ref[...] += jnp.concatenate([dsh, dsc], axis=0)
        dnw_ref[...] += jnp.sum(dh * xh * (1.0 + scale), axis=0, keepdims=True)

    row = pl.BlockSpec((TM, D), lambda i: (i, 0))
    vec = _full((1, D))
    return _pcall(body, name="mod2_bwd", grid=(s_len // TM,),
                  in_specs=[row, row, row, row, vec, _full((1, 2, D)), vec],
                  out_specs=[row, row, _full((2, D)), vec, vec],
                  out_shape=[jax.ShapeDtypeStruct((s_len, D), F32), jax.ShapeDtypeStruct((s_len, D), BF16),
                             jax.ShapeDtypeStruct((2, D), F32), jax.ShapeDtypeStruct((1, D), F32),
                             jax.ShapeDtypeStruct((1, D), F32)])(x1, dh2, dx2, ao, nw2, ss2, g1)


def _mod1_bwd(ctx, x, dh, dx1, nw1, ss1, carry=None):
    s_len = dx1.shape[0]
    tt = L + s_len

    def body(c_ref, x_ref, dh_ref, dx1_ref, nw_ref, ss_ref, dx_ref, dss_ref, dnw_ref):
        i = pl.program_id(0)
        tok = jnp.where(i == 0, c_ref[...], x_ref[...])

        @pl.when(i == 0)
        def _():
            dnw_ref[...] = jnp.zeros_like(dnw_ref)

        @pl.when(i <= 1)
        def _():
            dss_ref[...] = jnp.zeros_like(dss_ref)

        dh_ = dh_ref[...]
        nw = nw_ref[...]
        scale = ss_ref[0][1:2]
        dxn, xh = _norm_bwd_rows(tok, dh_, nw, scale)

        @pl.when(i >= 1)
        def _():
            dx_ref[...] = dx1_ref[...] + dxn

        dsh = jnp.sum(dh_, axis=0, keepdims=True)
        dsc = jnp.sum(dh_ * xh * nw, axis=0, keepdims=True)
        dss_ref[...] += jnp.concatenate([dsh, dsc], axis=0)[None]
        dnw_ref[...] += jnp.sum(dh_ * xh * (1.0 + scale), axis=0, keepdims=True)

    row = pl.BlockSpec((TM, D), lambda i: (i, 0))
    lat = pl.BlockSpec((TM, D), lambda i: (jnp.maximum(i - 1, 0), 0))
    sel = pl.BlockSpec((1, 2, D), lambda i: (jnp.minimum(i, 1), 0, 0))
    return _pcall(body, name="mod1_bwd", grid=(tt // TM,),
                  in_specs=_tok_specs() + [row, lat, _full((1, D)), sel],
                  out_specs=[lat, sel, _full((1, D))],
                  out_shape=[jax.ShapeDtypeStruct((s_len, D), F32), jax.ShapeDtypeStruct((2, 2, D), F32),
                             jax.ShapeDtypeStruct((1, D), F32)], carry=carry)(ctx, x, dh, dx1, nw1, ss1)


def _rows(c):
    return slice(c * CH, (c + 1) * CH)


def _chunk_masks(rev, transpose=False):
    r = lax.broadcasted_iota(jnp.int32, (TM, TM), 0)
    c = lax.broadcasted_iota(jnp.int32, (TM, TM), 1)
    same = (r // CH) == (c // CH)
    before = (c >= r) if (rev != transpose) else (c <= r)
    return same & before, same


def _chunk_scan(x, rev, transpose=False):
    r = lax.broadcasted_iota(jnp.int32, (CH, CH), 0)
    c = lax.broadcasted_iota(jnp.int32, (CH, CH), 1)
    tri = ((c >= r) if (rev != transpose) else (c <= r)).astype(F32)
    return jnp.concatenate([_dot(tri, x[_rows(ch)], prec=HI) for ch in range(x.shape[0] // CH)], axis=0)


def _chunk_total(x):
    return jnp.concatenate([jnp.broadcast_to(jnp.sum(x[_rows(ch)], axis=0, keepdims=True), (CH, x.shape[1]))
                            for ch in range(x.shape[0] // CH)], axis=0)


def _hgrn_gate(fl, qraw, lg):
    lb = 1.0 / (1.0 + jnp.exp(lg[1:2] - lg[0:1]))
    sg = _sig(fl)
    f = lb + (1.0 - lb) * sg
    q = qraw * _sig(qraw) * (HGD ** -0.5)
    return lb, sg, f, q


def _hgrn_fwd(p, lg, *, rev, carry=None, readout=None):
    tt = p.shape[0]
    nt = tt // TM
    ncht = TM // CH
    d = 1 if rev else 0

    def tile_of(s):
        return jnp.where(s == 0, 0, nt - s) if rev else s

    def body(*refs):
        if readout is None:
            f_ref, inp_ref, q_ref, lg_ref, o_ref, st_ref, state = refs
        else:
            f_ref, inp_ref, q_ref, lg_ref, oo_ref, g_ref, hw_ref, o_ref, st_ref, y_ref, state = refs
        s = pl.program_id(0)

        @pl.when(s == 0)
        def _():
            state[...] = jnp.zeros_like(state)

        _, _, f, q = _hgrn_gate(f_ref[...], q_ref[...], lg_ref[0])
        lf = jnp.log(f)
        causal, _ = _chunk_masks(rev)
        cum = _chunk_scan(lf, rev)
        tot = _chunk_total(lf)
        qd = (q * jnp.exp(cum)).astype(BF16)
        kd = ((1.0 - f) * jnp.exp(-cum)).astype(BF16)
        ke = ((1.0 - f) * jnp.exp(tot - cum)).astype(BF16)
        et = jnp.exp(tot)
        v = inp_ref[...].astype(BF16)
        order = range(ncht - 1, -1, -1) if rev else range(ncht)
        outs = []
        for h in range(4):
            sl = slice(h * HGD, (h + 1) * HGD)
            qd_, kd_, ke_, v_ = qd[:, sl], kd[:, sl], ke[:, sl], v[:, sl]
            pm = jnp.where(causal, _dot(qd_, kd_, NT), 0.0).astype(BF16)
            o_h = _dot(pm, v_)
            upd = [_dot(v_[_rows(c)], ke_[_rows(c)], TN) for c in range(ncht)]
            st = state[h]
            for c in order:
                st_ref[c, h] = st
                st = st * et[c * CH:c * CH + 1, sl] + upd[c]
            state[h] = st
            inter = [_dot(qd_[_rows(c)], st_ref[c, h].astype(BF16), NT) for c in range(ncht)]
            outs.append(o_h + jnp.concatenate(inter, axis=0))
        o_tile = jnp.concatenate(outs, axis=1)
        o_ref[...] = o_tile
        if readout is not None:
            @pl.when(tile_of(s) >= 1)
            def _():
                g = g_ref[...]
                y_ref[...] = (_head_rms(oo_ref[...] + o_tile, None, 4) * hw_ref[...] * (g * _sig(g))).astype(BF16)

    def col(cb):
        return pl.BlockSpec((TM, HGW), lambda s: (tile_of(s), cb))

    in_specs = [col(C_FB if rev else C_FF), col(C_INP), col(C_QHG), pl.BlockSpec((1, 2, HGW), lambda s: (d, 0, 0))]
    out_specs = [col(0), pl.BlockSpec((ncht, 4, HGD, HGD), lambda s: (tile_of(s), 0, 0, 0))]
    out_shape = [jax.ShapeDtypeStruct((tt, HGW), F32), jax.ShapeDtypeStruct((nt * ncht, 4, HGD, HGD), F32)]
    args = [p, p, p, lg]
    if readout is not None:
        in_specs += [col(0), col(C_GHG), _full((1, HGW))]
        args += [readout[0], p, readout[1]]
        assert rev
        out_specs.append(pl.BlockSpec((TM, HGW), lambda s: (jnp.where(s == 0, nt - 2, tile_of(s) - 1), 0)))
        out_shape.append(jax.ShapeDtypeStruct((tt - L, HGW), BF16))
    return _pcall(body, name="hgrn_fwd_rev" if rev else "hgrn_fwd", grid=(nt,), in_specs=in_specs,
                  out_specs=out_specs, out_shape=out_shape, scratch=[pltpu.VMEM((4, HGD, HGD), F32)],
                  carry=carry)(*args)


def _hgrn_bwd(p, lg, do, st, dp, prev, *, rev, carry=None):
    tt = p.shape[0]
    nt = tt // TM
    ncht = TM // CH
    d = 1 if rev else 0
    second = prev is not None

    def tile_of(s):
        return jnp.where(s == nt - 1, 0, s + 1) if rev else nt - 1 - s

    def body(*refs):
        if second:
            (f_ref, inp_ref, q_ref, lg_ref, do_ref, st_ref, dvp_ref, dqp_ref, _dp_in,
             dp_ref, dlg_ref, dstate) = refs
        else:
            (f_ref, inp_ref, q_ref, lg_ref, do_ref, st_ref, _dp_in,
             dp_ref, dv_ref, dq_ref, dlg_ref, dstate) = refs
        s = pl.program_id(0)
        tile = tile_of(s)

        @pl.when(s == 0)
        def _():
            dstate[...] = jnp.zeros_like(dstate)
            dlg_ref[...] = jnp.zeros_like(dlg_ref)

        qraw = q_ref[...]
        lb, sg, f, q = _hgrn_gate(f_ref[...], qraw, lg_ref[0])
        lf = jnp.log(f)
        causal, _ = _chunk_masks(rev)
        causal_t, _ = _chunk_masks(rev, transpose=True)
        cum = _chunk_scan(lf, rev)
        tot = _chunk_total(lf)
        ea, eb, ee, et = jnp.exp(cum), jnp.exp(-cum), jnp.exp(tot - cum), jnp.exp(tot)
        qdf, kdf, kef = q * ea, (1.0 - f) * eb, (1.0 - f) * ee
        qd, kd, ke = qdf.astype(BF16), kdf.astype(BF16), kef.astype(BF16)
        v = inp_ref[...].astype(BF16)
        dob = jnp.where(tile == 0, 0.0, do_ref[...]).astype(BF16)
        order = range(ncht) if rev else range(ncht - 1, -1, -1)
        dq_l, dk_l, dv_l, dcum_l, dtot_l = [], [], [], [], []
        for h in range(4):
            sl = slice(h * HGD, (h + 1) * HGD)
            qd_, kd_, ke_, v_, do_ = qd[:, sl], kd[:, sl], ke[:, sl], v[:, sl], dob[:, sl]
            pmt = jnp.where(causal_t, _dot(kd_, qd_, NT), 0.0).astype(BF16)
            dpm = jnp.where(causal, _dot(do_, v_, NT), 0.0).astype(BF16)
            dpmt = jnp.where(causal_t, _dot(v_, do_, NT), 0.0).astype(BF16)
            dv = _dot(pmt, do_)
            dqd = _dot(dpm, kd_)
            dkd = _dot(dpmt, qd_)
            upd = [_dot(do_[_rows(c)], qd_[_rows(c)], TN) for c in range(ncht)]
            ds = dstate[h]
            ds1 = [None] * ncht
            for c in order:
                ds1[c] = ds
                ds = ds * et[c * CH:c * CH + 1, sl] + upd[c]
            dstate[h] = ds
            dke_c, dv_c, dqd_c, dtot_c = [], [], [], []
            for c in range(ncht):
                st0 = st_ref[c, h]
                dsb = ds1[c].astype(BF16)
                dke_ = _dot(v_[_rows(c)], dsb)
                dke_c.append(dke_)
                dv_c.append(_dot(ke_[_rows(c)], dsb, NT))
                dqd_c.append(_dot(do_[_rows(c)], st0.astype(BF16)))
                dt = (jnp.sum(ds1[c] * st0, axis=0, keepdims=True) * et[c * CH:c * CH + 1, sl]
                      + jnp.sum(dke_ * kef[_rows(c), sl], axis=0, keepdims=True))
                dtot_c.append(jnp.broadcast_to(dt, (CH, HGD)))
            dke = jnp.concatenate(dke_c, axis=0)
            dqd = dqd + jnp.concatenate(dqd_c, axis=0)
            dv_l.append(dv + jnp.concatenate(dv_c, axis=0))
            dtot_l.append(jnp.concatenate(dtot_c, axis=0))
            dq_l.append(dqd * ea[:, sl])
            dk_l.append(dkd * eb[:, sl] + dke * ee[:, sl])
            dcum_l.append(dqd * qdf[:, sl] - dkd * kdf[:, sl] - dke * kef[:, sl])
        dcum = jnp.concatenate(dcum_l, axis=1)
        dlf = _chunk_scan(dcum, rev, transpose=True) + jnp.concatenate(dtot_l, axis=1)
        dq_t = jnp.concatenate(dq_l, axis=1)
        dv_t = jnp.concatenate(dv_l, axis=1)

        df = dlf / f - jnp.concatenate(dk_l, axis=1)
        dfl = df * (1.0 - lb) * sg * (1.0 - sg)
        dlb = jnp.sum(df * (1.0 - sg), axis=0, keepdims=True)
        dl0 = dlb * lb * (1.0 - lb)
        dlg_ref[...] += jnp.concatenate([dl0, -dl0], axis=0)[None]
        if second:
            sq = _sig(qraw)
            dqr = (dqp_ref[...] + dq_t) * (HGD ** -0.5) * (sq * (1.0 + qraw * (1.0 - sq)))
            dp_ref[...] = jnp.concatenate([dfl, dvp_ref[...] + dv_t, dqr], axis=1).astype(BF16)
        else:
            dp_ref[...] = dfl.astype(BF16)
            dv_ref[...] = dv_t
            dq_ref[...] = dq_t

    def col(cb):
        return pl.BlockSpec((TM, HGW), lambda s: (tile_of(s), cb))

    tok = pl.BlockSpec((TM, HGW), lambda s: (tile_of(s), 0))
    in_specs = [col(C_FB if rev else C_FF), col(C_INP), col(C_QHG),
                pl.BlockSpec((1, 2, HGW), lambda s: (d, 0, 0)),
                pl.BlockSpec((TM, HGW), lambda s: (jnp.maximum(tile_of(s) - 1, 0), 0)),
                pl.BlockSpec((ncht, 4, HGD, HGD), lambda s: (tile_of(s), 0, 0, 0))]
    args = [p, p, p, lg, do, st]
    dlg_spec = _full((1, 2, HGW))
    dlg_shape = jax.ShapeDtypeStruct((1, 2, HGW), F32)
    if second:
        in_specs += [tok, tok]
        args += [prev[0], prev[1]]
        out_specs = [pl.BlockSpec((TM, 3 * HGW), lambda s: (tile_of(s), 0)), dlg_spec]
        out_shape = [jax.ShapeDtypeStruct(dp.shape, BF16), dlg_shape]
    else:
        out_specs = [pl.BlockSpec((TM, HGW), lambda s: (tile_of(s), C_FB if rev else C_FF)), tok, tok, dlg_spec]
        out_shape = [jax.ShapeDtypeStruct(dp.shape, BF16), jax.ShapeDtypeStruct((tt, HGW), F32),
                     jax.ShapeDtypeStruct((tt, HGW), F32), dlg_shape]
    in_specs.append(ANY)
    args.append(dp)
    return _pcall(body, name="hgrn_bwd_rev" if rev else "hgrn_bwd", grid=(nt,),
                  in_specs=in_specs, out_specs=out_specs, out_shape=out_shape,
                  scratch=[pltpu.VMEM((4, HGD, HGD), F32)],
                  aliases={len(args) - 1: 0}, carry=carry)(*args)


def _head_rms(o, w, nheads):
    outs = []
    for h in range(nheads):
        oh = o[:, h * HGD:(h + 1) * HGD]
        outs.append(oh * lax.rsqrt(jnp.mean(oh * oh, axis=-1, keepdims=True) + EPS))
    return jnp.concatenate(outs, axis=1)


def _readout(o0, o1, p, hw4):
    s_len = o0.shape[0] - L

    def body(o0_ref, o1_ref, g_ref, w_ref, y_ref):
        xh = _head_rms(o0_ref[...] + o1_ref[...], None, 4)
        g = g_ref[...]
        y_ref[...] = (xh * w_ref[...] * (g * _sig(g))).astype(BF16)

    lat = pl.BlockSpec((TM, HGW), lambda i: (i + 1, 0))
    return _pcall(body, name="readout", grid=(s_len // TM,),
                  in_specs=[lat, lat, pl.BlockSpec((TM, HGW), lambda i: (i + 1, C_GHG)), _full((1, HGW))],
                  out_specs=pl.BlockSpec((TM, HGW), lambda i: (i, 0)),
                  out_shape=jax.ShapeDtypeStruct((s_len, HGW), BF16))(o0, o1, p, hw4)


def _readout_bwd(o0, o1, p, hw4, dy, dp, carry=None):
    tt = o0.shape[0]
    s_len = tt - L

    def body(o0_ref, o1_ref, g_ref, w_ref, dy_ref, _dp_in, dp_ref, do_ref, dw_ref):
        i = pl.program_id(0)

        @pl.when(i == 0)
        def _():
            dw_ref[...] = jnp.zeros_like(dw_ref)
            dp_ref[...] = jnp.zeros_like(dp_ref)

        @pl.when(i >= 1)
        def _():
            o = o0_ref[...] + o1_ref[...]
            g = g_ref[...]
            w = w_ref[...]
            sg = _sig(g)
            dy_ = dy_ref[...]
            dsw = dy_ * (g * sg)
            outs, xhs = [], []
            for h in range(4):
                sl = slice(h * HGD, (h + 1) * HGD)
                oh = o[:, sl]
                r = lax.rsqrt(jnp.mean(oh * oh, axis=-1, keepdims=True) + EPS)
                xh = oh * r
                dxh = dsw[:, sl] * w[:, sl]
                outs.append(r * (dxh - xh * jnp.mean(dxh * xh, axis=-1, keepdims=True)))
                xhs.append(xh)
            xh = jnp.concatenate(xhs, axis=1)
            do_ref[...] = jnp.concatenate(outs, axis=1)
            dp_ref[...] = (dy_ * xh * w * (sg * (1.0 + g * (1.0 - sg)))).astype(BF16)
            dw_ref[...] += jnp.sum(dsw * xh, axis=0, keepdims=True)

    tok = pl.BlockSpec((TM, HGW), lambda i: (i, 0))
    lat = pl.BlockSpec((TM, HGW), lambda i: (jnp.maximum(i - 1, 0), 0))
    return _pcall(body, name="readout_bwd", grid=(tt // TM,),
                  in_specs=[tok, tok, pl.BlockSpec((TM, HGW), lambda i: (i, C_GHG)), _full((1, HGW)), lat, ANY],
                  out_specs=[pl.BlockSpec((TM, HGW), lambda i: (i, C_GHG)), lat, _full((1, HGW))],
                  out_shape=[jax.ShapeDtypeStruct(dp.shape, BF16), jax.ShapeDtypeStruct((s_len, HGW), F32),
                             jax.ShapeDtypeStruct((1, HGW), F32)],
                  aliases={5: 0}, carry=carry)(o0, o1, p, hw4, dy, dp)


def _rope_tables(s_len):
    t = np.arange(s_len)
    inv = ROPE_THETA ** (-np.arange(0, 32, 2, dtype=np.float64) / 32)
    def half(pos):
        ang = pos[:, None].astype(np.float64) * inv[None, :]
        return (np.concatenate([np.cos(ang), np.cos(ang)], 1), np.concatenate([-np.sin(ang), np.sin(ang)], 1))
    cr, sr = half(t // GRID_W)
    cc, sc = half(t % GRID_W)
    cos = np.concatenate([cr, cc, cr, cc], 1)
    sin = np.concatenate([sr, sc, sr, sc], 1)
    cos = np.concatenate([np.ones((L, 128)), cos], 0)
    sin = np.concatenate([np.zeros((L, 128)), sin], 0)
    return jnp.asarray(cos, F32), jnp.asarray(sin, F32)


def _blockdiag(n, w):
    i = np.arange(n)
    return jnp.asarray((i[:, None] // w == i[None, :] // w) / float(w), F32)


def _dup_matrix():
    m = np.zeros((128, 512), np.float32)
    for g in range(2):
        for j in range(4):
            for dd in range(HDIM):
                m[64 * g + dd, 256 * g + 64 * j + dd] = 1.0
    return m


def _head_mean(x, blockdiag):
    return _dot(x, blockdiag, prec=lax.Precision.HIGH)


def _rot(x):
    n = x.shape[1]
    lane = lax.broadcasted_iota(jnp.int32, x.shape, 1)
    return jnp.where((lane % 32) < 16, pltpu.roll(x, n - 16, 1), pltpu.roll(x, 16, 1))


def _qk_prep(p, cos, sin, qnw8, knw2, bd512, bd128, dup):
    tt = p.shape[0]

    def body(q_ref, kv_ref, cos_ref, sin_ref, qw_ref, kw_ref, b5_ref, b1_ref, dup_ref,
             qr_ref, k4_ref, v4_ref):
        cos_, sin_ = cos_ref[...], sin_ref[...]
        q = q_ref[...]
        qn = q * lax.rsqrt(_head_mean(q * q, b5_ref[...]) + EPS) * qw_ref[...]
        cos4 = jnp.concatenate([cos_] * 4, axis=1)
        sin4 = jnp.concatenate([sin_] * 4, axis=1)
        qr_ref[...] = ((qn * cos4 + _rot(qn) * sin4) * (HDIM ** -0.5)).astype(BF16)
        kv = kv_ref[...]
        k, v = kv[:, :128], kv[:, 128:]
        kn = k * lax.rsqrt(_head_mean(k * k, b1_ref[...]) + EPS) * kw_ref[...]
        kr = kn * cos_ + _rot(kn) * sin_
        k4_ref[...] = _bdot(kr, dup_ref[...]).astype(BF16)
        v4_ref[...] = _bdot(v, dup_ref[...]).astype(BF16)

    row = lambda w, cb: pl.BlockSpec((TM, w), lambda i: (i, cb))
    out = jax.ShapeDtypeStruct((tt, ATW), BF16)
    return _pcall(body, name="qk_prep", grid=(tt // TM,),
                  in_specs=[row(ATW, C_QRAW), row(256, C_KV), row(128, 0), row(128, 0),
                            _full((1, ATW)), _full((1, 128)), _full((ATW, ATW)), _full((128, 128)),
                            _full((128, ATW))],
                  out_specs=[row(ATW, 0)] * 3, out_shape=[out] * 3)(
                      p, p, cos, sin, qnw8, knw2, bd512, bd128, dup)


def _attn_masks(i, nb):
    r = lax.broadcasted_iota(jnp.int32, (4 * BLK, 3 * BLK + L), 0) % BLK
    c = lax.broadcasted_iota(jnp.int32, (4 * BLK, 3 * BLK + L), 1)
    kpos = (i - 1) * BLK + c
    loc = (jnp.abs(c - BLK - r) <= BLK) & (kpos >= 0) & (kpos < nb * BLK)
    return loc | (c >= 3 * BLK)


def _stack_mask():
    r = lax.broadcasted_iota(jnp.int32, (4 * BLK, 256), 0)
    lane = lax.broadcasted_iota(jnp.int32, (4 * BLK, 256), 1)
    return (r // BLK) == (lane // HDIM)


def _stack_heads(xg, fill=0.0):
    x4 = jnp.concatenate([xg] * 4, axis=0)
    return jnp.where(_stack_mask(), x4, jnp.full_like(x4, fill))


def _unstack_heads(x4):
    out = jnp.where(_lane_mask(0), x4[0:BLK], 0.0)
    for j in range(1, 4):
        out = out + jnp.where(_lane_mask(j), x4[j * BLK:(j + 1) * BLK], 0.0)
    return out


def _per_head_rows(vals):
    return jnp.concatenate([jnp.broadcast_to(v, (BLK, 1)) for v in vals], axis=0)


def _lane_mask(j):
    lane = lax.broadcasted_iota(jnp.int32, (1, 256), 1)
    return (lane // HDIM) == j


def _attn_specs(nb):
    blk = lambda off: pl.BlockSpec((BLK, ATW), lambda i: (jnp.clip(i + off, 0, nb - 1) + 2, 0))
    ctx = pl.BlockSpec((L, ATW), lambda i: (0, 0))
    return blk, ctx


def _attn_fwd(qr, k4, v4, sinks, carry=None):
    tt = qr.shape[0]
    s_len = tt - L
    nb = s_len // BLK

    def body(sk_ref, q_ref, kp, ko, kn, kc, vp, vo, vn, vc, y_ref, lse_ref):
        i = pl.program_id(0)
        valid = _attn_masks(i, nb)
        q = q_ref[...]
        ys, lses = [], []
        for g in range(2):
            gs = slice(256 * g, 256 * g + 256)
            kcat = jnp.concatenate([kp[:, gs], ko[:, gs], kn[:, gs], kc[:, gs]], axis=0)
            vcat = jnp.concatenate([vp[:, gs], vo[:, gs], vn[:, gs], vc[:, gs]], axis=0)
            sink4 = _per_head_rows([sk_ref[4 * g + j] for j in range(4)])
            q4 = _stack_heads(q[:, gs])
            o_parts, l_parts = [], []
            for hp in range(2):
                rows = slice(2 * BLK * hp, 2 * BLK * (hp + 1))
                sink = sink4[rows]
                s = jnp.where(valid[rows], _dot(q4[rows], kcat, NT), -1e30)
                m = jnp.maximum(jnp.max(s, axis=-1, keepdims=True), sink)
                e = jnp.exp(s - m)
                den = jnp.sum(e, axis=-1, keepdims=True) + jnp.exp(sink - m)
                o_parts.append(_bdot(e * (1.0 / den), vcat))
                l_parts.append(jnp.broadcast_to(m + jnp.log(den), (2 * BLK, 256)))
            ys.append(_unstack_heads(jnp.concatenate(o_parts, axis=0)))
            lses.append(_unstack_heads(jnp.concatenate(l_parts, axis=0)))
        y_ref[...] = jnp.concatenate(ys, axis=1).astype(BF16)
        lse_ref[...] = jnp.concatenate(lses, axis=1)

    blk, ctx = _attn_specs(nb)
    out = pl.BlockSpec((BLK, ATW), lambda i: (i, 0))
    return _pcall(body, name="attn_fwd", grid=(nb,),
                  in_specs=[pl.BlockSpec(memory_space=pltpu.SMEM), blk(0),
                            blk(-1), blk(0), blk(1), ctx, blk(-1), blk(0), blk(1), ctx],
                  out_specs=[out, out],
                  out_shape=[jax.ShapeDtypeStruct((s_len, ATW), BF16),
                             jax.ShapeDtypeStruct((s_len, ATW), F32)], carry=carry)(
                      sinks, qr, k4, k4, k4, k4, v4, v4, v4, v4)


def _attn_bwd(qr, k4, v4, sinks, y, lse, dy, carry=None):
    tt = qr.shape[0]
    s_len = tt - L
    nb = s_len // BLK

    def body(sk_ref, q_ref, kp, ko, kn, kc, vp, vo, vn, vc, y_ref, lse_ref, dy_ref,
             dq_ref, dkw_ref, dvw_ref, dkc_ref, dvc_ref, dsk_ref):
        i = pl.program_id(0)

        @pl.when(i == 0)
        def _():
            dkc_ref[...] = jnp.zeros_like(dkc_ref)
            dvc_ref[...] = jnp.zeros_like(dvc_ref)
            dsk_ref[...] = jnp.zeros_like(dsk_ref)

        valid = _attn_masks(i, nb)
        q = q_ref[...]
        dy_ = dy_ref[...]
        dly = dy_ * y_ref[...].astype(F32)
        lse_ = lse_ref[...]
        dqs = []
        for g in range(2):
            gs = slice(256 * g, 256 * g + 256)
            kcat = jnp.concatenate([kp[:, gs], ko[:, gs], kn[:, gs], kc[:, gs]], axis=0)
            vcat = jnp.concatenate([vp[:, gs], vo[:, gs], vn[:, gs], vc[:, gs]], axis=0)
            q4 = _stack_heads(q[:, gs])
            dy4 = _stack_heads(dy_[:, gs]).astype(BF16)
            lse4 = jnp.max(_stack_heads(lse_[:, gs], fill=-1e30), axis=-1, keepdims=True)
            delta = jnp.sum(_stack_heads(dly[:, gs]), axis=-1, keepdims=True)
            sink = _per_head_rows([sk_ref[4 * g + j] for j in range(4)])
            pr = jnp.where(valid, jnp.exp(_dot(q4, kcat, NT) - lse4), 0.0)
            dsb = (pr * (_dot(dy4, vcat, NT) - delta)).astype(BF16)
            dsink = jnp.exp(sink - lse4) * delta
            for j in range(4):
                dsk_ref[4 * g + j:4 * g + j + 1, :] += jnp.broadcast_to(
                    -jnp.sum(dsink[j * BLK:(j + 1) * BLK], axis=0, keepdims=True), (1, 128))
            dqs.append(_unstack_heads(_dot(dsb, kcat)))
            dkg = _dot(dsb, q4, TN)
            dvg = _dot(pr.astype(BF16), dy4, TN)
            dkw_ref[0, :, gs] = dkg[:3 * BLK]
            dvw_ref[0, :, gs] = dvg[:3 * BLK]
            dkc_ref[:, gs] += dkg[3 * BLK:]
            dvc_ref[:, gs] += dvg[3 * BLK:]
        dq_ref[...] = jnp.concatenate(dqs, axis=1)

    blk, ctx = _attn_specs(nb)
    out = pl.BlockSpec((BLK, ATW), lambda i: (i, 0))
    win = pl.BlockSpec((1, 3 * BLK, ATW), lambda i: (i, 0, 0))
    acc = _full((L, ATW))
    return _pcall(body, name="attn_bwd", grid=(nb,),
                  in_specs=[pl.BlockSpec(memory_space=pltpu.SMEM), blk(0),
                            blk(-1), blk(0), blk(1), ctx, blk(-1), blk(0), blk(1), ctx, out, out, out],
                  out_specs=[out, win, win, acc, acc, _full((8, 128))],
                  out_shape=[jax.ShapeDtypeStruct((s_len, ATW), F32),
                             jax.ShapeDtypeStruct((nb, 3 * BLK, ATW), F32),
                             jax.ShapeDtypeStruct((nb, 3 * BLK, ATW), F32),
                             jax.ShapeDtypeStruct((L, ATW), F32), jax.ShapeDtypeStruct((L, ATW), F32),
                             jax.ShapeDtypeStruct((8, 128), F32)], carry=carry)(
                      sinks, qr, k4, k4, k4, k4, v4, v4, v4, v4, y, lse, dy)


def _attn_post(p, cos, sin, qnw8, knw2, bd512, bd128, dupt, dq, dkw, dvw, dkc, dvc, dp, carry=None):
    tt = p.shape[0]
    s_len = tt - L
    nb = s_len // BLK
    nctx = L // BLK

    def body(q_ref, kv_ref, cos_ref, sin_ref, qw_ref, kw_ref, b5_ref, b1_ref, dupt_ref,
             dq_ref, kwp, kwo, kwn, vwp, vwo, vwn, dkc_ref, dvc_ref, _dp_in,
             dp_ref, dqw_ref, dkw_ref):
        t = pl.program_id(0)
        j = t - nctx

        @pl.when(t == 0)
        def _():
            dqw_ref[...] = jnp.zeros_like(dqw_ref)
            dkw_ref[...] = jnp.zeros_like(dkw_ref)

        is_lat = t >= nctx
        cos_, sin_ = cos_ref[...], sin_ref[...]
        has_p = is_lat & (j >= 1)
        has_n = is_lat & (j <= nb - 2)
        dk4 = (jnp.where(is_lat, kwo[0], dkc_ref[...]) + jnp.where(has_p, kwp[0], 0.0)
               + jnp.where(has_n, kwn[0], 0.0))
        dv4 = (jnp.where(is_lat, vwo[0], dvc_ref[...]) + jnp.where(has_p, vwp[0], 0.0)
               + jnp.where(has_n, vwn[0], 0.0))
        dkr = _dot(dk4, dupt_ref[...], prec=HI)
        dv = _dot(dv4, dupt_ref[...], prec=HI)
        kv = kv_ref[...]
        k = kv[:, :128]
        kw = kw_ref[...]
        rk = lax.rsqrt(_head_mean(k * k, b1_ref[...]) + EPS)
        xk = k * rk
        dkn = dkr * cos_ + _rot(dkr * sin_)
        dxk = dkn * kw
        dk = rk * (dxk - xk * _head_mean(dxk * xk, b1_ref[...]))
        dkw_ref[...] += jnp.sum(dkn * xk, axis=0, keepdims=True)
        q = q_ref[...]
        qw = qw_ref[...]
        rq = lax.rsqrt(_head_mean(q * q, b5_ref[...]) + EPS)
        xq = q * rq
        cos4 = jnp.concatenate([cos_] * 4, axis=1)
        sin4 = jnp.concatenate([sin_] * 4, axis=1)
        dqr = jnp.where(is_lat, dq_ref[...], 0.0) * (HDIM ** -0.5)
        dqn = dqr * cos4 + _rot(dqr * sin4)
        dxq = dqn * qw
        dqraw = rq * (dxq - xq * _head_mean(dxq * xq, b5_ref[...]))
        dqw_ref[...] += jnp.sum(dqn * xq, axis=0, keepdims=True)
        dp_ref[...] = jnp.concatenate([dqraw, dk, dv], axis=1).astype(BF16)

    row = lambda w, cb: pl.BlockSpec((BLK, w), lambda t: (t, cb))
    lat = pl.BlockSpec((BLK, ATW), lambda t: (jnp.maximum(t - nctx, 0), 0))

    def part(off):
        return pl.BlockSpec((1, BLK, ATW), lambda t: (jnp.clip(t - nctx + off, 0, nb - 1), 1 - off, 0))

    cacc = pl.BlockSpec((BLK, ATW), lambda t: (jnp.minimum(t, nctx - 1), 0))
    return _pcall(body, name="attn_post", grid=(tt // BLK,),
                  in_specs=[row(ATW, C_QRAW), row(256, C_KV), row(128, 0), row(128, 0),
                            _full((1, ATW)), _full((1, 128)), _full((ATW, ATW)), _full((128, 128)),
                            _full((ATW, 128)), lat, part(-1), part(0), part(1), part(-1), part(0), part(1),
                            cacc, cacc, ANY],
                  out_specs=[pl.BlockSpec((BLK, 768), lambda t: (t, C_QKV)), _full((1, ATW)), _full((1, 128))],
                  out_shape=[jax.ShapeDtypeStruct(dp.shape, BF16), jax.ShapeDtypeStruct((1, ATW), F32),
                             jax.ShapeDtypeStruct((1, 128), F32)],
                  aliases={18: 0}, carry=carry)(p, p, cos, sin, qnw8, knw2, bd512, bd128, dupt,
                                   dq, dkw, dkw, dkw, dvw, dvw, dvw, dkc, dvc, dp)


def _branch_merge(y_hg, y_at, bh4, ba4, p):
    s_len = y_hg.shape[0]

    def body(yh_ref, ya_ref, bh_ref, ba_ref, gh_ref, ga_ref, ah_ref, aa_ref, m_ref):
        yh, ya = yh_ref[...], ya_ref[...]
        ah = jnp.concatenate([_bdot(yh, bh_ref[j]) for j in range(4)], axis=1)
        aa = jnp.concatenate([_bdot(ya, ba_ref[j]) for j in range(4)], axis=1)
        ah_ref[...] = ah.astype(BF16)
        aa_ref[...] = aa.astype(BF16)
        m_ref[...] = (_sig(gh_ref[...]) * ah + _sig(ga_ref[...]) * aa).astype(BF16)

    row = pl.BlockSpec((TM, D), lambda i: (i, 0))
    y = pl.BlockSpec((TM, HGW), lambda i: (i, 0))
    f = jax.ShapeDtypeStruct((s_len, D), BF16)
    return _pcall(body, name="branch_merge", grid=(s_len // TM,),
                  in_specs=[y, y, _full(bh4.shape), _full(ba4.shape),
                            pl.BlockSpec((TM, D), lambda i: (i + 1, 2)), pl.BlockSpec((TM, D), lambda i: (i + 1, 3))],
                  out_specs=[row, row, row],
                  out_shape=[f, f, jax.ShapeDtypeStruct((s_len, D), BF16)])(y_hg, y_at, bh4, ba4, p, p)


def _branch_bwd(dmh, dma, bh4, ba4, y_hg, y_at):
    s_len = dmh.shape[0]
    nk = s_len // TS
    ns = D // 4

    def body(dh_ref, da_ref, bh_ref, ba_ref, yh_ref, ya_ref, dyh_ref, dya_ref, gh_ref, ga_ref, acc_h, acc_a):
        t = pl.program_id(0)

        @pl.when(t == 0)
        def _():
            acc_h[...] = jnp.zeros_like(acc_h)
            acc_a[...] = jnp.zeros_like(acc_a)

        for d_ref, w_ref, y_ref, dy_ref, acc in ((dh_ref, bh_ref, yh_ref, dyh_ref, acc_h),
                                                 (da_ref, ba_ref, ya_ref, dya_ref, acc_a)):
            y = y_ref[...]
            dy = jnp.zeros((TS, HGW), F32)
            for j in range(4):
                dj = d_ref[:, j * ns:(j + 1) * ns]
                dy = dy + _bdot(dj, w_ref[j], NT)
                acc[j] += _bdot(y, dj, TN)
            dy_ref[...] = dy

        @pl.when(t == nk - 1)
        def _():
            gh_ref[...] = acc_h[...].astype(BF16)
            ga_ref[...] = acc_a[...].astype(BF16)

    dm = pl.BlockSpec((TS, D), lambda t: (t, 0))
    y = pl.BlockSpec((TS, HGW), lambda t: (t, 0))
    w = _full(bh4.shape)
    fy = jax.ShapeDtypeStruct((s_len, HGW), F32)
    gw = jax.ShapeDtypeStruct(bh4.shape, BF16)
    return _pcall(body, name="branch_bwd", grid=(nk,), in_specs=[dm, dm, w, w, y, y],
                  out_specs=[y, y, w, w], out_shape=[fy, fy, gw, gw],
                  scratch=[pltpu.VMEM(bh4.shape, F32)] * 2)(dmh, dma, bh4, ba4, y_hg, y_at)


def _merge_bwd(dattn, w_o, mixed, ah, aa, p, carry=None):
    tt = p.shape[0]
    s_len = tt - L
    nt = tt // TM

    def body(da_ref, wo_ref, mx_ref, ah_ref, aa_ref, gh_ref, ga_ref, dp_ref, dmh_ref, dma_ref, go_ref, acc):
        i = pl.program_id(0)

        @pl.when(i == 0)
        def _():
            dp_ref[...] = jnp.zeros_like(dp_ref)
            acc[...] = jnp.zeros_like(acc)

        @pl.when(i >= 1)
        def _():
            da = da_ref[...]
            acc[...] += _bdot(mx_ref[...], da, TN)
            dm_ = _bdot(da, wo_ref[...], NT)
            sh, sa = _sig(gh_ref[...]), _sig(ga_ref[...])
            dp_ref[...] = jnp.concatenate([dm_ * ah_ref[...].astype(F32) * sh * (1.0 - sh),
                                           dm_ * aa_ref[...].astype(F32) * sa * (1.0 - sa)], axis=1).astype(BF16)
            dmh_ref[...] = (dm_ * sh).astype(BF16)
            dma_ref[...] = (dm_ * sa).astype(BF16)

        @pl.when(i == nt - 1)
        def _():
            go_ref[...] = acc[...].astype(BF16)

    lat = pl.BlockSpec((TM, D), lambda i: (jnp.maximum(i - 1, 0), 0))
    return _pcall(body, name="merge_bwd", grid=(nt,),
                  in_specs=[lat, _full((D, D)), lat, lat, lat, pl.BlockSpec((TM, D), lambda i: (i, 2)),
                            pl.BlockSpec((TM, D), lambda i: (i, 3))],
                  out_specs=[pl.BlockSpec((TM, 2 * D), lambda i: (i, C_GATES)), lat, lat, _full((D, D))],
                  out_shape=[jax.ShapeDtypeStruct((tt, NCOL), BF16), jax.ShapeDtypeStruct((s_len, D), BF16),
                             jax.ShapeDtypeStruct((s_len, D), BF16), jax.ShapeDtypeStruct((D, D), BF16)],
                  scratch=[pltpu.VMEM((D, D), F32)], carry=carry)(dattn, w_o, mixed, ah, aa, p, p)


def _local_step(x, ctx, tgt, mod, modc, nw1, nw2, lg, hw, qnw, knw, sinks,
                w_in, wts, dist=None):
    s_len = x.shape[0]
    tt = s_len + L
    ss1 = jnp.stack([modc, mod[0:2]])
    ss2 = mod[3:5][None]
    g1, g2 = mod[2:3], mod[5:6]
    hw4 = jnp.tile(hw, (1, 4))
    qnw8 = jnp.tile(qnw, (1, 8))
    knw2 = jnp.tile(knw, (1, 2))
    cos, sin = _rope_tables(s_len)
    bd512, bd128 = _blockdiag(ATW, HDIM), _blockdiag(128, HDIM)
    dupm = _dup_matrix()
    dup, dupt = jnp.asarray(dupm, BF16), jnp.asarray(dupm.T, F32)
    tmt = tt

    def four(b):
        return b.reshape(4, 2 * b.shape[1], b.shape[2])

    def halves(g):
        return g.reshape(4, 2, g.shape[1] // 2, g.shape[2])

    h = _mod1(ctx, x, nw1, ss1)
    if dist is None:
        bh4, ba4, w_o, g4, u4, dn4 = wts
        p = _mm_in(h, w_in, tmt)
        o0, st0 = _hgrn_fwd(p, lg, rev=False)
        o1, st1, y_hg = _hgrn_fwd(p, lg, rev=True, readout=(o0, hw4))
    else:
        core, chip = dist
        half = wts[3].shape[1] // 2
        p, first = _mm_in(h, w_in, tmt, carry=_carry_join(_carry_gather(list(wts[0:3])),
                                                          _carry_gather([wts[3]], rows=[(0, half)])))
        (o0, st0), (g8,) = _hgrn_fwd(p, lg, rev=False, carry=_carry_gather([first[3]], rows=[(half, half)]))
        (o1, st1, y_hg), (dn8a,) = _hgrn_fwd(p, lg, rev=True, readout=(o0, hw4),
                                             carry=_carry_gather([wts[5]], rows=[(0, half)]))
        bh4, ba4, w_o, g4 = four(first[0]), four(first[1]), four(first[2]).reshape(D, D), four(g8)
    qr, k4, v4 = _qk_prep(p, cos, sin, qnw8, knw2, bd512, bd128, dup)
    if dist is None:
        y_at, lse = _attn_fwd(qr, k4, v4, sinks)
    else:
        (y_at, lse), (u8, dn8) = _attn_fwd(qr, k4, v4, sinks,
                                           carry=_carry_gather([wts[4], dn8a], rows=[None, (half, half)]))
        u4, dn4 = four(u8), four(dn8)
    ah, aa, mixed = _branch_merge(y_hg, y_at, bh4, ba4, p)
    ao, x1, h2 = _out_proj_mod2(mixed, w_o, x, g1, nw2, ss2)
    a4, b4, z4 = _ffn_up(h2, g4, u4)
    sq, dx2, dyb, dg2 = _ffn_down_loss(z4, dn4, x1, g2, tgt)

    da4, db4 = _ffn_dz(dyb, dn4, a4, b4)
    g_dn = _ffn_gdn(z4, dyb)
    if dist is None:
        dh2 = _ffn_dh2(da4, db4, g4, u4)
    else:
        dn_units = [halves(g_dn)]
        dh2, dn_recv = _ffn_dh2(da4, db4, g4, u4, carry=_carry_pairx(dn_units))
        dn_pairs = _rs_pair_add(dn_units, dn_recv, core)
    if dist is None:
        g_g, g_u = _ffn_ggu(h2, da4, db4)
    else:
        (g_g, g_u), c_dn = _ffn_ggu(h2, da4, db4, carry=_carry_chipx(dn_pairs))
        red_dn = _rs_chip_add(dn_pairs, c_dn, core, chip)
    dx1, dattn, dss2, dnw2, dg1 = _mod2_bwd(x1, dh2, dx2, ao, nw2, ss2, g1)
    if dist is None:
        dp, dmh, dma, g_o = _merge_bwd(dattn, w_o, mixed, ah, aa, p)
    else:
        gu_units = [halves(g_g), halves(g_u)]
        (dp, dmh, dma, g_o), gu_recv = _merge_bwd(dattn, w_o, mixed, ah, aa, p, carry=_carry_pairx(gu_units))
        ffn_pairs = list(dn_pairs) + list(_rs_pair_add(gu_units, gu_recv, core))
    dy_hg, dy_at, g_bh, g_ba = _branch_bwd(dmh, dma, bh4, ba4, y_hg, y_at)
    if dist is None:
        dp, do, dhw4 = _readout_bwd(o0, o1, p, hw4, dy_hg, dp)
        dq, dkw, dvw, dkc, dvc, dsk = _attn_bwd(qr, k4, v4, sinks, y_at, lse, dy_at)
        dp, dqnw8, dknw2 = _attn_post(p, cos, sin, qnw8, knw2, bd512, bd128, dupt, dq, dkw, dvw, dkc, dvc, dp)
    else:
        mix_units = [halves(g_bh), halves(g_ba), halves(g_o.reshape(4, D // 4, D))]
        (dp, do, dhw4), mix_recv = _readout_bwd(o0, o1, p, hw4, dy_hg, dp, carry=_carry_pairx(mix_units))
        mix_pairs = _rs_pair_add(mix_units, mix_recv, core)
        (dq, dkw, dvw, dkc, dvc, dsk), bwd = _attn_bwd(
            qr, k4, v4, sinks, y_at, lse, dy_at,
            carry=_carry_join(_carry_chipx(ffn_pairs[1:2]), _carry_sibx(red_dn)))
        red_g = _rs_chip_add(ffn_pairs[1:2], bwd[0:1], core, chip)
        (dp, dqnw8, dknw2), post = _attn_post(
            p, cos, sin, qnw8, knw2, bd512, bd128, dupt, dq, dkw, dvw, dkc, dvc, dp, carry=_carry_sibx(red_g))
    if dist is None:
        dp, dv0, dq0, dlg0 = _hgrn_bwd(p, lg, do, st0, dp, None, rev=False)
        dp, dlg1 = _hgrn_bwd(p, lg, do, st1, dp, (dv0, dq0), rev=True)
    else:
        (dp, dv0, dq0, dlg0), c_u = _hgrn_bwd(p, lg, do, st0, dp, None, rev=False,
                                              carry=_carry_chipx(ffn_pairs[2:3]))
        red_u = _rs_chip_add(ffn_pairs[2:3], c_u, core, chip)
        (dp, dlg1), last = _hgrn_bwd(p, lg, do, st1, dp, (dv0, dq0), rev=True,
                                     carry=_carry_join(_carry_chipx(mix_pairs), _carry_sibx(red_u)))
        mix_reds = _rs_chip_add(mix_pairs, last[0:3], core, chip)
        ffn_done = bwd[1:2] + post[0:1] + last[3:4]
    g_in = _mm_gin(dp, h, tmt)
    if dist is None:
        dh = _mm_dh(dp, w_in, tmt)
        gx, dss1, dnw1 = _mod1_bwd(ctx, x, dh, dx1, nw1, ss1)
        rs = None
    else:
        in_units = [halves(g_in.reshape(4, NCOL // 4, D))]
        dh, both = _mm_dh(dp, w_in, tmt, carry=_carry_join(_carry_pairx(in_units), _carry_sibx(mix_reds)))
        in_recv, mix_done = both[0:1], both[1:4]
        in_pairs = _rs_pair_add(in_units, in_recv, core)
        first_rows = (0, in_pairs[0].shape[1] // 2)
        (gx, dss1, dnw1), in_part = _mod1_bwd(ctx, x, dh, dx1, nw1, ss1,
                                              carry=_carry_chipx(in_pairs, rows=first_rows))
        rs = dict(ffn_done=ffn_done, mix_done=mix_done, in_pairs=in_pairs, in_part=in_part)

    dmod = jnp.concatenate([dss1[1], dg1, dss2, dg2], axis=0)
    dmodc = dss1[0]
    raw = (dss1, dg1, dss2, dg2, dnw1, dnw2, dhw4, dqnw8, dknw2, dsk, dlg0, dlg1)
    small = dict(raw=raw, dmod=dmod, dmodc=dmodc, dnw1=dnw1, dnw2=dnw2,
                 dhw=dhw4.reshape(4, HGD).sum(0, keepdims=True),
                 dqnw=dqnw8.reshape(8, HDIM).sum(0, keepdims=True),
                 dknw=dknw2.reshape(2, HDIM).sum(0, keepdims=True),
                 dsinks=dsk[:, 0], dlg=jnp.concatenate([dlg0, dlg1], axis=0))
    big = dict(w_in=g_in, w_bh=g_bh, w_ba=g_ba, w_o=g_o, w_g=g_g, w_u=g_u, w_dn=g_dn)
    return sq, gx, big, small, rs


def _place():
    x, y, c = lax.axis_index("x"), lax.axis_index("y"), lax.axis_index("c")
    return x, y, c


def _gather_blocks(x_refs, out_refs, send_sems, recv_sems, local_sems):
    n = len(out_refs)
    x, y, c = _place()
    me, sibling = (x, y, c), (x, y, 1 - c)
    chips = [(1 - x, y), (x, 1 - y), (1 - x, 1 - y)]

    def slot(u, px, py, pc):
        return out_refs[u].at[4 * px + 2 * py + pc]

    def copy(u, k, block, to, src=None):
        return pltpu.make_async_remote_copy(
            src_ref=slot(u, *block) if src is None else src, dst_ref=slot(u, *block),
            send_sem=send_sems.at[u, k], recv_sem=recv_sems.at[u, k], device_id=to, device_id_type=MESH)

    mines = []
    if x_refs is not None:
        mines = [pltpu.make_async_copy(x_refs[u], slot(u, *me), local_sems.at[u]) for u in range(n)]
    for cp in mines:
        cp.start()
    first = []
    for u in range(n):
        src = None if x_refs is None else x_refs[u]
        first.append(copy(u, 0, me, sibling, src=src))
        first += [copy(u, 1 + j, me, (*chip, c), src=src) for j, chip in enumerate(chips)]
    for cp in first:
        cp.start()
    passed = []
    for j, chip in enumerate(chips):
        for u in range(n):
            copy(u, 1 + j, (*chip, c), me).wait_recv()
            fwd = copy(u, 4 + j, (*chip, c), sibling)
            fwd.start()
            passed.append(fwd)
    for u in range(n):
        copy(u, 0, sibling, me).wait_recv()
    for j, chip in enumerate(chips):
        for u in range(n):
            copy(u, 4 + j, (*chip, 1 - c), me).wait_recv()
    for cp in first + passed:
        cp.wait_send()
    for cp in mines:
        cp.wait()


def _gather_sems(n):
    return [pltpu.SemaphoreType.DMA((n, 7)), pltpu.SemaphoreType.DMA((n, 7)), pltpu.SemaphoreType.DMA((n,))]


def _allgather(blks, *, name, in_vmem):
    n = len(blks)
    space = pltpu.VMEM if in_vmem else pl.ANY

    def body(*refs):
        _gather_blocks(refs[:n], refs[n:2 * n], *refs[2 * n:])

    return pl.pallas_call(
        body, name=name, out_shape=[jax.ShapeDtypeStruct((8,) + b.shape, b.dtype) for b in blks],
        in_specs=[pl.BlockSpec(memory_space=space)] * n, out_specs=[pl.BlockSpec(memory_space=space)] * n,
        scratch_shapes=_gather_sems(n))(*blks)


def _cast_place(ws, c, dev):
    n = len(ws)

    def body(s_ref, *refs):
        for u in range(n):
            refs[n + u][0] = refs[u][...].astype(BF16)

    in_specs, out_specs, out_shape = [], [], []
    for w in ws:
        q, cols = w.shape[0] // 4, w.shape[1]
        in_specs.append(pl.BlockSpec((q, cols), lambda i, s: (2 * s[0] + i, 0)))
        out_specs.append(pl.BlockSpec((1, q, cols), lambda i, s: (s[1], i, 0)))
        out_shape.append(jax.ShapeDtypeStruct((8, 2 * q, cols), BF16))
    return pl.pallas_call(
        body, name="cast_place",
        grid_spec=pltpu.PrefetchScalarGridSpec(num_scalar_prefetch=1, grid=(2,), in_specs=in_specs,
                                               out_specs=out_specs),
        out_shape=_out_hbm(out_shape),
        compiler_params=pltpu.CompilerParams(vmem_limit_bytes=48 << 20))(jnp.stack([c, dev]), *_in_hbm(ws))


def _gather_phases(out_refs, send_sems, recv_sems, rows=None):
    n = len(out_refs)
    x, y, c = _place()
    me, sibling = (x, y, c), (x, y, 1 - c)
    chips = [(1 - x, y), (x, 1 - y), (1 - x, 1 - y)]

    def copy(u, k, block, to):
        px, py, pc = block
        ref = out_refs[u].at[4 * px + 2 * py + pc]
        if rows is not None and rows[u] is not None:
            ref = ref.at[pl.ds(rows[u][0], rows[u][1])]
        return pltpu.make_async_remote_copy(src_ref=ref, dst_ref=ref, send_sem=send_sems.at[u, k],
                                            recv_sem=recv_sems.at[u, k], device_id=to, device_id_type=MESH)

    def start():
        for u in range(n):
            copy(u, 0, me, sibling).start()
            for j, chip in enumerate(chips):
                copy(u, 1 + j, me, (*chip, c)).start()

    def mid():
        for j, chip in enumerate(chips):
            for u in range(n):
                copy(u, 1 + j, (*chip, c), me).wait_recv()
                copy(u, 4 + j, (*chip, c), sibling).start()

    def end():
        for u in range(n):
            copy(u, 0, sibling, me).wait_recv()
        for j, chip in enumerate(chips):
            for u in range(n):
                copy(u, 4 + j, (*chip, 1 - c), me).wait_recv()
        for u in range(n):
            copy(u, 0, me, sibling).wait_send()
            for j, chip in enumerate(chips):
                copy(u, 1 + j, me, (*chip, c)).wait_send()
                copy(u, 4 + j, (*chip, c), sibling).wait_send()

    return start, mid, end


def _carry_gather(bufs, rows=None):
    n = len(bufs)
    return _Carry(bufs, [jax.ShapeDtypeStruct(b.shape, b.dtype) for b in bufs], {u: u for u in range(n)},
                  [pltpu.SemaphoreType.DMA((n, 7)), pltpu.SemaphoreType.DMA((n, 7))],
                  lambda ins, outs, sems: _gather_phases(outs, *sems, rows=rows), "both")


def _allgather_inplace(bufs, *, name):
    n = len(bufs)

    def body(*refs):
        for phase in _gather_phases(refs[n:2 * n], *refs[2 * n:]):
            phase()

    return pl.pallas_call(
        body, name=name, out_shape=[jax.ShapeDtypeStruct(b.shape, b.dtype) for b in bufs],
        in_specs=[ANY] * n, out_specs=[ANY] * n, input_output_aliases={u: u for u in range(n)},
        scratch_shapes=[pltpu.SemaphoreType.DMA((n, 7)), pltpu.SemaphoreType.DMA((n, 7))])(*bufs)


def _ag_small(raw, sq):
    def body(dss1, dg1, dss2, dg2, dnw1, dnw2, dhw4, dqnw8, dknw2, dsk, dlg0, dlg1, sq_ref,
             out_ref, tot_ref, blk, send_sems, recv_sems, local_sems):
        blk[...] = jnp.zeros_like(blk)
        blk[0:2, :] = dss1[1]
        blk[2:3, :] = dg1[...]
        blk[3:5, :] = dss2[...]
        blk[5:6, :] = dg2[...]
        blk[6:8, :] = dss1[0]
        blk[8:9, :] = dnw1[...]
        blk[9:10, :] = dnw2[...]
        blk[10:11, 0:HGW] = dhw4[...]
        blk[10:11, HGW:D] = dqnw8[...]
        blk[11:12, 0:128] = dknw2[...]
        blk[12:14, 0:HGW] = dlg0[0]
        blk[14:16, 0:HGW] = dlg1[0]
        blk[16:24, 0:128] = dsk[...]
        blk[24:25, :] = sq_ref[...]
        _gather_blocks([blk], [out_ref], send_sems, recv_sems, local_sems)
        acc = out_ref[0]
        for i in range(1, 8):
            acc = acc + out_ref[i]
        tot_ref[...] = acc

    vm = pl.BlockSpec(memory_space=pltpu.VMEM)
    return pl.pallas_call(
        body, name="ag_small",
        out_shape=[jax.ShapeDtypeStruct((8, 32, D), F32), jax.ShapeDtypeStruct((32, D), F32)],
        in_specs=[vm] * 13, out_specs=[vm, vm],
        scratch_shapes=[pltpu.VMEM((32, D), F32)] + _gather_sems(1))(*raw, sq)


def _rs_pair_exchange(units):
    n = len(units)

    def body(*refs):
        start, _, end = _pairx_phases(refs[:n], refs[n:2 * n], *refs[2 * n:])
        start()
        end()

    return pl.pallas_call(
        body, name="rs_pair_exchange", out_shape=_pairx_shapes(units),
        in_specs=[ANY] * n, out_specs=[ANY] * n,
        scratch_shapes=[pltpu.SemaphoreType.DMA((n, 4)), pltpu.SemaphoreType.DMA((n, 4))])(*units)


def _pairx_shapes(units):
    return [jax.ShapeDtypeStruct((4,) + g.shape[2:], g.dtype) for g in units]


def _pairx_phases(g_refs, r_refs, send_sems, recv_sems):
    n = len(g_refs)
    x, y, c = _place()
    cps = [pltpu.make_async_remote_copy(
        src_ref=g_refs[u].at[j, 1 - c], dst_ref=r_refs[u].at[j], send_sem=send_sems.at[u, j],
        recv_sem=recv_sems.at[u, j], device_id=(x, y, 1 - c), device_id_type=MESH)
        for u in range(n) for j in range(4)]

    def start():
        for cp in cps:
            cp.start()

    def end():
        for cp in cps:
            cp.wait()

    return start, None, end


def _carry_pairx(units):
    n = len(units)
    return _Carry(units, _pairx_shapes(units), {},
                  [pltpu.SemaphoreType.DMA((n, 4)), pltpu.SemaphoreType.DMA((n, 4))],
                  lambda ins, outs, sems: _pairx_phases(ins, outs, *sems), "sib")


def _rs_pair_add(units, recvs, c):
    n = len(units)

    def body(c_ref, *refs):
        for u in range(n):
            refs[2 * n + u][...] = (refs[u][0].astype(F32) + refs[n + u][...].astype(F32)).astype(BF16)

    in_specs, out_specs, out_shape = [], [], []
    for g in units:
        h, w = g.shape[2] // 2, g.shape[3]
        in_specs.append(pl.BlockSpec((1, 1, h, w), lambda j, i, cr: (j, cr[0], i, 0)))
    for g in units:
        h, w = g.shape[2] // 2, g.shape[3]
        in_specs.append(pl.BlockSpec((1, h, w), lambda j, i, cr: (j, i, 0)))
        out_specs.append(pl.BlockSpec((1, h, w), lambda j, i, cr: (j, i, 0)))
        out_shape.append(jax.ShapeDtypeStruct((4, 2 * h, w), BF16))
    return pl.pallas_call(
        body, name="rs_pair_add",
        grid_spec=pltpu.PrefetchScalarGridSpec(num_scalar_prefetch=1, grid=(4, 2), in_specs=in_specs,
                                               out_specs=out_specs),
        out_shape=_out_hbm(out_shape),
        compiler_params=pltpu.CompilerParams(vmem_limit_bytes=48 << 20))(
            c.reshape(1), *_in_hbm(list(units) + list(recvs)))


def _rs_chip_exchange(pairs):
    n = len(pairs)

    def body(*refs):
        start, _, end = _chipx_phases(refs[:n], refs[n:2 * n], *refs[2 * n:])
        start()
        end()

    return pl.pallas_call(
        body, name="rs_chip_exchange", out_shape=[jax.ShapeDtypeStruct(p.shape, p.dtype) for p in pairs],
        in_specs=[ANY] * n, out_specs=[ANY] * n,
        scratch_shapes=[pltpu.SemaphoreType.DMA((n, 3)), pltpu.SemaphoreType.DMA((n, 3))])(*pairs)


def _chipx_phases(p_refs, r_refs, send_sems, recv_sems, rows=None):
    n = len(p_refs)
    x, y, c = _place()
    k = 2 * x + y

    def part(ref):
        return ref if rows is None else ref.at[pl.ds(rows[0], rows[1])]

    sends = []
    for d in range(1, 4):
        j = (k + d) % 4
        for u in range(n):
            sends.append(pltpu.make_async_remote_copy(
                src_ref=part(p_refs[u].at[j]), dst_ref=part(r_refs[u].at[k]), send_sem=send_sems.at[u, d - 1],
                recv_sem=recv_sems.at[u, d - 1], device_id=(j // 2, j % 2, c), device_id_type=MESH))

    def start():
        for cp in sends:
            cp.start()

    def end():
        for d in range(1, 4):
            src = (k + 4 - d) % 4
            for u in range(n):
                pltpu.make_async_remote_copy(
                    src_ref=part(p_refs[u].at[src]), dst_ref=part(r_refs[u].at[src]),
                    send_sem=send_sems.at[u, d - 1], recv_sem=recv_sems.at[u, d - 1], device_id=(x, y, c),
                    device_id_type=MESH).wait_recv()
        for cp in sends:
            cp.wait_send()

    return start, None, end


def _carry_chipx(pairs, rows=None, into=None):
    n = len(pairs)
    sems = [pltpu.SemaphoreType.DMA((n, 3)), pltpu.SemaphoreType.DMA((n, 3))]
    shapes = [jax.ShapeDtypeStruct(p.shape, p.dtype) for p in pairs]
    if into is None:
        return _Carry(pairs, shapes, {}, sems, lambda ins, outs, s: _chipx_phases(ins, outs, *s, rows=rows),
                      "chips")
    return _Carry(list(pairs) + list(into), shapes, {n + u: u for u in range(n)}, sems,
                  lambda ins, outs, s: _chipx_phases(ins[:n], outs, *s, rows=rows), "chips")


def _rs_chip_add(pairs, contribs, c, chip):
    n = len(pairs)

    def body(s_ref, *refs):
        for u in range(n):
            a, b, c_, d = refs[4 * u:4 * u + 4]
            refs[4 * n + u][0] = ((a[0].astype(F32) + b[0].astype(F32)) + c_[0].astype(F32)) + d[0].astype(F32)

    in_specs, out_specs, out_shape, args = [], [], [], []
    for p, r in zip(pairs, contribs):
        h, w = p.shape[1] // 2, p.shape[2]
        in_specs += [pl.BlockSpec((1, h, w), functools.partial(lambda d, i, s: ((s[1] + d) % 4, i, 0), d))
                     for d in range(4)]
        args += [p, r, r, r]
        out_specs.append(pl.BlockSpec((1, h, w), lambda i, s: (s[0], i, 0)))
        out_shape.append(jax.ShapeDtypeStruct((2, 2 * h, w), F32))
    return pl.pallas_call(
        body, name="rs_chip_add",
        grid_spec=pltpu.PrefetchScalarGridSpec(num_scalar_prefetch=1, grid=(2,), in_specs=in_specs,
                                               out_specs=out_specs),
        out_shape=_out_hbm(out_shape),
        compiler_params=pltpu.CompilerParams(vmem_limit_bytes=48 << 20))(jnp.stack([c, chip]), *_in_hbm(args))


def _rs_sibling_gather(reds):
    n = len(reds)

    def body(*refs):
        start, _, end = _sibx_phases(refs[n:2 * n], *refs[2 * n:])
        start()
        end()

    return pl.pallas_call(
        body, name="rs_sibling_gather", out_shape=[jax.ShapeDtypeStruct(r.shape, r.dtype) for r in reds],
        in_specs=[ANY] * n, out_specs=[ANY] * n, input_output_aliases={u: u for u in range(n)},
        scratch_shapes=[pltpu.SemaphoreType.DMA((n,))] * 2)(*reds)


def _sibx_phases(o_refs, send_sems, recv_sems):
    n = len(o_refs)
    x, y, c = _place()
    cps = [pltpu.make_async_remote_copy(
        src_ref=o_refs[u].at[c], dst_ref=o_refs[u].at[c], send_sem=send_sems.at[u], recv_sem=recv_sems.at[u],
        device_id=(x, y, 1 - c), device_id_type=MESH) for u in range(n)]

    def start():
        for cp in cps:
            cp.start()

    def end():
        for u in range(n):
            cps[u].wait_send()
            pltpu.make_async_remote_copy(
                src_ref=o_refs[u].at[1 - c], dst_ref=o_refs[u].at[1 - c], send_sem=send_sems.at[u],
                recv_sem=recv_sems.at[u], device_id=(x, y, 1 - c), device_id_type=MESH).wait_recv()

    return start, None, end


def _carry_sibx(reds):
    n = len(reds)
    return _Carry(reds, [jax.ShapeDtypeStruct(r.shape, r.dtype) for r in reds], {u: u for u in range(n)},
                  [pltpu.SemaphoreType.DMA((n,))] * 2, lambda ins, outs, sems: _sibx_phases(outs, *sems), "sib")


def _prologue(blk, c_ctx, w, b, in8):
    n = w.shape[1]

    def body(blk_ref, cctx_ref, w_ref, b_ref, _in_in, g0_ref, c16_ref, g1_ref, in_ref, mod_s,
             s1, r1, l1, s2, r2, l2, s3, r3):
        start, mid, end = _gather_phases([in_ref], s3, r3)
        _gather_blocks([blk_ref], [g0_ref], s1, r1, l1)
        start()
        c16 = jnp.concatenate([g0_ref[i, 0:1, :] for i in range(8)] + [cctx_ref[...], jnp.zeros((7, D), F32)],
                              axis=0)
        c16_ref[...] = c16
        mod_s[...] = _dot(c16 * _sig(c16), w_ref[...], prec=HI) + b_ref[...]
        _gather_blocks([mod_s], [g1_ref], s2, r2, l2)
        mid()
        end()

    vm = pl.BlockSpec(memory_space=pltpu.VMEM)
    return pl.pallas_call(
        body, name="prologue",
        out_shape=[jax.ShapeDtypeStruct((8, 8, D), F32), jax.ShapeDtypeStruct((16, D), F32),
                   jax.ShapeDtypeStruct((8, 16, n), F32), jax.ShapeDtypeStruct(in8.shape, in8.dtype)],
        in_specs=[vm, vm, vm, vm, ANY], out_specs=[vm, vm, vm, ANY], input_output_aliases={4: 3},
        scratch_shapes=[pltpu.VMEM((16, n), F32)] + _gather_sems(1) + _gather_sems(1)
        + [pltpu.SemaphoreType.DMA((1, 7)), pltpu.SemaphoreType.DMA((1, 7))],
        compiler_params=pltpu.CompilerParams(vmem_limit_bytes=48 << 20))(blk, c_ctx, w, b, in8)


def _ada_bwd(c16, dmod16, w, carry=None):
    n = w.shape[1]
    tn = 512

    def body(c_ref, d_ref, w_ref, gw_ref, gc_ref):
        j = pl.program_id(0)

        @pl.when(j == 0)
        def _():
            gc_ref[...] = jnp.zeros_like(gc_ref)

        cc = c_ref[...]
        dm = d_ref[...]
        gw_ref[...] = _dot(cc * _sig(cc), dm, TN, prec=HI)
        gc_ref[...] += _dot(dm, w_ref[...], NT, prec=HI)

    return _pcall(body, name="ada_bwd", grid=(n // tn,),
                  in_specs=[_full((16, D)), pl.BlockSpec((16, tn), lambda j: (0, j)),
                            pl.BlockSpec((D, tn), lambda j: (0, j))],
                  out_specs=[pl.BlockSpec((D, tn), lambda j: (0, j)), _full((16, D))],
                  out_shape=[jax.ShapeDtypeStruct((D, n), F32),
                             jax.ShapeDtypeStruct((16, D), F32)], carry=carry)(c16, dmod16, w)


def _adam_math(w, g, m, v):
    c1 = 1.0 - ADAM_B1 ** ADAM_STEP
    c2 = 1.0 - ADAM_B2 ** ADAM_STEP
    nm = ADAM_B1 * m + (1.0 - ADAM_B1) * g
    nv = ADAM_B2 * v + (1.0 - ADAM_B2) * (g * g)
    return -ADAM_LR * ((nm / c1) / (jnp.sqrt(nv / c2) + ADAM_EPS) + ADAM_WD * w), nm, nv


def _adamw_small(ws, gs, ms, vs):
    n = len(ws)

    def body(*refs):
        for u in range(n):
            d_, nm, nv = _adam_math(refs[u][...], refs[n + u][...], refs[2 * n + u][...], refs[3 * n + u][...])
            refs[4 * n + u][...] = d_
            refs[5 * n + u][...] = nm
            refs[6 * n + u][...] = nv

    specs = [_full(w.shape) for w in ws]
    shapes = [jax.ShapeDtypeStruct(w.shape, F32) for w in ws]
    out = _pcall(body, name="adamw_small", grid=(1,), in_specs=specs * 4, out_specs=specs * 3,
                 out_shape=shapes * 3)(*ws, *gs, *ms, *vs)
    return out[:n], out[n:2 * n], out[2 * n:]


def _cctx_grad(parts, c_ctx):
    def body(p_ref, c_ref, o_ref):
        acc = p_ref[0:1, :]
        for k in range(1, 4):
            acc = acc + p_ref[k:k + 1, :]
        cc = c_ref[...]
        s = _sig(cc)
        o_ref[...] = acc * (s * (1.0 + cc * (1.0 - s)))

    return _pcall(body, name="cctx_grad", grid=(1,), in_specs=[_full(parts.shape), _full((1, D))],
                  out_specs=_full((1, D)), out_shape=jax.ShapeDtypeStruct((1, D), F32))(parts, c_ctx)


ADAM_STEPS = 8


def _adamw_multi(ws, gs, ms, vs, *, name, carry=None):
    n = len(ws)

    def body(*refs):
        for u in range(n):
            refs[4 * n + u][...], refs[5 * n + u][...], refs[6 * n + u][...] = _adam_math(
                refs[u][...], refs[n + u][...], refs[2 * n + u][...], refs[3 * n + u][...])

    specs = [pl.BlockSpec((w.shape[0] // ADAM_STEPS, w.shape[1]), lambda i: (i, 0)) for w in ws]
    shapes = [jax.ShapeDtypeStruct(w.shape, F32) for w in ws]
    res = _pcall(body, name=name, grid=(ADAM_STEPS,), in_specs=specs * 4, out_specs=specs * 3,
                 out_shape=shapes * 3, carry=carry)(*ws, *gs, *ms, *vs)
    out, extra = res if carry is not None else (res, None)
    return (out[:n], out[n:2 * n], out[2 * n:]), extra


def kernel(x, c, ctx, c_ctx, w_ada, b_ada, norm_mix_w, norm_ffn_w, w_in, hgrn_lb_logits, hgrn_norm_w, q_norm_w, k_norm_w, attn_sinks, w_branch_hgrn, w_branch_attn, w_out, w_ffn_gate, w_ffn_up, w_ffn_down, loss_target, m_c_ctx, m_w_ada, m_b_ada, m_norm_mix_w, m_norm_ffn_w, m_w_in, m_hgrn_lb_logits, m_hgrn_norm_w, m_q_norm_w, m_k_norm_w, m_attn_sinks, m_w_branch_hgrn, m_w_branch_attn, m_w_out, m_w_ffn_gate, m_w_ffn_up, m_w_ffn_down, v_c_ctx, v_w_ada, v_b_ada, v_norm_mix_w, v_norm_ffn_w, v_w_in, v_hgrn_lb_logits, v_hgrn_norm_w, v_q_norm_w, v_k_norm_w, v_attn_sinks, v_w_branch_hgrn, v_w_branch_attn, v_w_out, v_w_ffn_gate, v_w_ffn_up, v_w_ffn_down):
    xi, yi, ci = _place()
    chip = 2 * xi + yi
    dev = 2 * chip + ci
    s_len = x.shape[1]

    shards = [w_in[0].T, w_branch_hgrn[0], w_branch_attn[0], w_out[0], w_ffn_gate[0].T, w_ffn_up[0].T,
              w_ffn_down[0]]
    bufs = _cast_place(shards, ci, dev)

    lbrow = jnp.pad(hgrn_lb_logits.reshape(1, 512), ((0, 0), (0, D - 512)))
    blk = jnp.concatenate([c, lbrow, jnp.zeros((6, D), F32)], axis=0)
    nada = w_ada.shape[2]
    b_sh = lax.dynamic_slice(b_ada, (0, chip * nada), (1, nada))
    g0, c16, g1, in8 = _prologue(blk, c_ctx[None], w_ada[0], b_sh, bufs[0])
    lg = g0[0::2, 1, :512].reshape(4, 2, 2, 128).transpose(1, 2, 0, 3).reshape(2, 2, HGW)
    modall = g1[0::2].transpose(1, 0, 2).reshape(16, 4 * nada)
    mod = lax.dynamic_slice(modall, (dev, 0), (1, 6 * D)).reshape(6, D)
    modc = modall[8].reshape(6, D)[:2]

    sq, gx, _, small, rs = _local_step(
        x[0], ctx[0], loss_target[0], mod, modc, norm_mix_w, norm_ffn_w, lg, hgrn_norm_w, q_norm_w,
        k_norm_w, attn_sinks[0], in8.reshape(NCOL, D), bufs[1:], dist=(ci, chip))

    def whole(r):
        return r.reshape(2 * r.shape[1], r.shape[2])

    g_dn, g_g, g_u = [whole(r) for r in rs["ffn_done"]]
    g_bh, g_ba, g_o = [whole(r) for r in rs["mix_done"]]
    in_pairs = rs["in_pairs"]
    rest_rows = (in_pairs[0].shape[1] // 2, in_pairs[0].shape[1] // 2)

    g2, tot = _ag_small(small["raw"], sq)
    loss = 0.5 * jnp.sum(tot[24]) / D
    dmodc_tot = jnp.pad(tot[6:8].reshape(1, 2 * D), ((0, 0), (0, 4 * D)))
    g_b_ada = tot[0:6].reshape(1, 6 * D) + dmodc_tot
    dmod16 = jnp.concatenate([g2[:, 0:6].reshape(8, 6 * D), dmodc_tot, jnp.zeros((7, 6 * D), F32)], axis=0)
    (g_w_ada, gc_part), in_contribs = _ada_bwd(
        c16, lax.dynamic_slice(dmod16, (0, chip * nada), (16, nada)), w_ada[0],
        carry=_carry_chipx(in_pairs, rows=rest_rows, into=rs["in_part"]))
    g3, = _allgather([gc_part[8:16]], name="ag_cctx", in_vmem=True)
    g_c_ctx = _cctx_grad(g3[0::2, 0], c_ctx[None])[0]
    g_nw1 = tot[8:9]
    g_nw2 = tot[9:10]
    g_hw = tot[10, :HGW].reshape(4, HGD).sum(0, keepdims=True)
    g_qnw = tot[10, HGW:].reshape(8, HDIM).sum(0, keepdims=True)
    g_knw = tot[11, :128].reshape(2, HDIM).sum(0, keepdims=True)
    g_sinks = tot[16:24, 0][None]
    g_lg = lax.dynamic_slice(tot[12:16, :HGW].reshape(2, 2, HGW), (0, 0, chip * 128), (2, 2, 128))

    names = ["c_ctx", "w_ada", "b_ada", "norm_mix_w", "norm_ffn_w", "w_in", "hgrn_lb_logits", "hgrn_norm_w",
             "q_norm_w", "k_norm_w", "attn_sinks", "w_branch_hgrn", "w_branch_attn", "w_out", "w_ffn_gate",
             "w_ffn_up", "w_ffn_down"]
    ws = dict(zip(names, [c_ctx, w_ada, b_ada, norm_mix_w, norm_ffn_w, w_in, hgrn_lb_logits, hgrn_norm_w,
                          q_norm_w, k_norm_w, attn_sinks, w_branch_hgrn, w_branch_attn, w_out, w_ffn_gate,
                          w_ffn_up, w_ffn_down]))
    ms = dict(zip(names, [m_c_ctx, m_w_ada, m_b_ada, m_norm_mix_w, m_norm_ffn_w, m_w_in, m_hgrn_lb_logits,
                          m_hgrn_norm_w, m_q_norm_w, m_k_norm_w, m_attn_sinks, m_w_branch_hgrn,
                          m_w_branch_attn, m_w_out, m_w_ffn_gate, m_w_ffn_up, m_w_ffn_down]))
    vs = dict(zip(names, [v_c_ctx, v_w_ada, v_b_ada, v_norm_mix_w, v_norm_ffn_w, v_w_in, v_hgrn_lb_logits,
                          v_hgrn_norm_w, v_q_norm_w, v_k_norm_w, v_attn_sinks, v_w_branch_hgrn,
                          v_w_branch_attn, v_w_out, v_w_ffn_gate, v_w_ffn_up, v_w_ffn_down]))
    transposed = ("w_in", "w_ffn_gate", "w_ffn_up")

    def view(a, n):
        return a[0].T if n in transposed else a[0]

    def unview(a, n):
        return a.T[None] if n in transposed else a[None]

    delta, new_m, new_v, grads = {}, {}, {}, {}

    def big_adamw(group, gs, name, carry=None):
        (d_, m_, v_), extra = _adamw_multi([view(ws[n], n) for n in group], gs, [view(ms[n], n) for n in group],
                                           [view(vs[n], n) for n in group], name=name, carry=carry)
        for i, n in enumerate(group):
            grads[n], delta[n], new_m[n], new_v[n] = (unview(gs[i], n), unview(d_[i], n), unview(m_[i], n),
                                                      unview(v_[i], n))
        return extra

    big_adamw(["w_ffn_down", "w_ffn_gate", "w_ffn_up", "w_out", "w_branch_hgrn", "w_branch_attn"],
              [g_dn, g_g, g_u, g_o, g_bh, g_ba], "adamw_first")
    in_reds = _rs_chip_add(in_pairs, in_contribs, ci, chip)
    g_in, = [whole(r) for r in _rs_sibling_gather(in_reds)]
    big_adamw(["w_in", "w_ada"], [g_in, g_w_ada], "adamw_second")
    grads.update(c_ctx=g_c_ctx, b_ada=g_b_ada, norm_mix_w=g_nw1, norm_ffn_w=g_nw2, hgrn_lb_logits=g_lg,
                 hgrn_norm_w=g_hw, q_norm_w=g_qnw, k_norm_w=g_knw, attn_sinks=g_sinks)
    small_names = [n for n in names if n not in delta]

    def two_d(a):
        return a.reshape(1, -1) if a.ndim == 1 else a

    sd, sm_, sv = _adamw_small(*[[two_d(d[n]) for n in small_names] for d in (ws, grads, ms, vs)])
    for i, n in enumerate(small_names):
        for dst, src in ((delta, sd), (new_m, sm_), (new_v, sv)):
            dst[n] = src[i].reshape(ws[n].shape)
    return (loss, gx[None], *[grads[n] for n in names], *[delta[n] for n in names],
            *[new_m[n] for n in names], *[new_v[n] for n in names])
```

```python
import functools

import numpy as np
import jax
import jax.numpy as jnp
from jax import lax
from jax.experimental import pallas as pl
from jax.experimental.pallas import tpu as pltpu

F32 = jnp.float32
BF16 = jnp.bfloat16
HI = lax.Precision.HIGHEST
MESH = pl.DeviceIdType.MESH

D = 1024
L = 256
TM = 256
HGW = 512
HGD = 128
CH = 32
ATW = 512
HDIM = 64
BLK = 128
GRID_W = 64
DFF = 2816
NCOL = 5376
EPS = 1e-6
ROPE_THETA = 10000.0

C_FB, C_INP, C_QHG, C_FF = 0, 1, 2, 3
C_GATES = 1
C_GHG, C_QRAW = 8, 9
C_KV = 20
C_QKV = 6

ADAM_LR, ADAM_B1, ADAM_B2, ADAM_EPS, ADAM_WD, ADAM_STEP = 0.001, 0.9, 0.999, 1e-08, 0.01, 10

NN = (((1,), (0,)), ((), ()))
NT = (((1,), (1,)), ((), ()))
TN = (((0,), (0,)), ((), ()))


def _dot(a, b, dims=NN, prec=None):
    return lax.dot_general(a, b, dims, precision=prec, preferred_element_type=F32)


def _bdot(a, b, dims=NN):
    return _dot(a.astype(BF16), b.astype(BF16), dims)


def _sig(x):
    return 1.0 / (1.0 + jnp.exp(-x))


class _Carry:
    def __init__(self, ins, outs, aliases, scratch, phases, peers):
        self.ins, self.outs, self.aliases, self.scratch, self.phases = ins, outs, aliases, scratch, phases
        self.peers = peers


BARRIER_IDS = {"sib": 1, "chips": 2, "both": 3}


def _peer_barrier(kind):
    x, y, c = _place()
    peers = []
    if kind in ("sib", "both"):
        peers.append((x, y, 1 - c))
    if kind in ("chips", "both"):
        peers += [(1 - x, y, c), (x, 1 - y, c), (1 - x, 1 - y, c)]
    bar = pltpu.get_barrier_semaphore()
    for peer in peers:
        pl.semaphore_signal(bar, inc=1, device_id=peer, device_id_type=MESH)
    pl.semaphore_wait(bar, len(peers))


def _in_hbm(args):
    return [pltpu.with_memory_space_constraint(a, pltpu.HBM) for a in args]


def _out_hbm(shapes):
    if isinstance(shapes, (list, tuple)):
        return [pltpu.HBM(s.shape, s.dtype) for s in shapes]
    return pltpu.HBM(shapes.shape, shapes.dtype)


def _carry_join(a, b):
    na_in, na_out, na_sc = len(a.ins), len(a.outs), len(a.scratch)
    aliases = dict(a.aliases)
    aliases.update({na_in + i: na_out + o for i, o in b.aliases.items()})

    def phases(ins, outs, sems):
        pa = a.phases(ins[:na_in], outs[:na_out], sems[:na_sc])
        pb = b.phases(ins[na_in:], outs[na_out:], sems[na_sc:])

        def both(fa, fb):
            if fa is None and fb is None:
                return None

            def run():
                for fn in (fa, fb):
                    if fn is not None:
                        fn()
            return run

        return tuple(both(fa, fb) for fa, fb in zip(pa, pb))

    return _Carry(list(a.ins) + list(b.ins), list(a.outs) + list(b.outs), aliases,
                  list(a.scratch) + list(b.scratch), phases, a.peers if a.peers == b.peers else "both")


def _pcall(body, *, name, grid, in_specs, out_specs, out_shape, scratch=(), aliases=None, vmem_mb=48,
           carry=None):
    params = pltpu.CompilerParams(dimension_semantics=("arbitrary",) * len(grid),
                                  vmem_limit_bytes=vmem_mb << 20)
    if carry is None:
        plain = pl.pallas_call(
            body, name=name, grid=grid, in_specs=in_specs, out_specs=out_specs, out_shape=_out_hbm(out_shape),
            scratch_shapes=list(scratch), input_output_aliases=aliases or {}, compiler_params=params)
        return lambda *args: plain(*_in_hbm(args))
    single = not isinstance(out_shape, (list, tuple))
    out_specs_l = [out_specs] if single else list(out_specs)
    out_shape_l = [out_shape] if single else list(out_shape)
    n_in, n_out, n_sc = len(in_specs), len(out_shape_l), len(scratch)
    k_in, k_out = len(carry.ins), len(carry.outs)
    nsteps = int(np.prod(grid))
    assert nsteps >= 3

    def wrapped(*refs):
        ins, cins = refs[:n_in], refs[n_in:n_in + k_in]
        o0 = n_in + k_in
        outs, couts = refs[o0:o0 + n_out], refs[o0 + n_out:o0 + n_out + k_out]
        s0 = o0 + n_out + k_out
        sc, csc = refs[s0:s0 + n_sc], refs[s0 + n_sc:]
        step = pl.program_id(0)
        for ax in range(1, len(grid)):
            step = step * grid[ax] + pl.program_id(ax)
        start, mid, end = carry.phases(cins, couts, csc)

        @pl.when(step == 0)
        def _():
            _peer_barrier(carry.peers)
            start()

        body(*ins, *outs, *sc)
        if mid is not None:
            pl.when(step == nsteps - 2)(mid)
        pl.when(step == nsteps - 1)(end)

    all_aliases = dict(aliases or {})
    all_aliases.update({n_in + i: n_out + o for i, o in carry.aliases.items()})
    call = pl.pallas_call(
        wrapped, name=name, grid=grid, in_specs=list(in_specs) + [ANY] * k_in,
        out_specs=out_specs_l + [ANY] * k_out, out_shape=_out_hbm(out_shape_l + list(carry.outs)),
        scratch_shapes=list(scratch) + list(carry.scratch), input_output_aliases=all_aliases,
        compiler_params=pltpu.CompilerParams(dimension_semantics=("arbitrary",) * len(grid),
                                             vmem_limit_bytes=vmem_mb << 20,
                                             collective_id=BARRIER_IDS[carry.peers]))

    def run(*args):
        res = call(*_in_hbm(args), *carry.ins)
        core = res[:n_out]
        return (core[0] if single else list(core)), list(res[n_out:])

    return run


def _full(shape):
    nd = len(shape)
    return pl.BlockSpec(shape, lambda *_: (0,) * nd)


ANY = pl.BlockSpec(memory_space=pl.ANY)


NT_IN = NCOL // 256


def _src_block(j):
    return j + jnp.where(j < 4, 2, jnp.where(j < 6, 3, jnp.where(j < 8, -6, jnp.where(
        j < 16, 5, jnp.where(j < 20, -7, -14)))))


def _mm_in(h, wt, tm, carry=None):
    tt = h.shape[0]

    def body(h_ref, w_ref, o_ref):
        o_ref[...] = _bdot(h_ref[...], w_ref[...], NT)

    return _pcall(body, name="mm_in", grid=(tt // tm, NT_IN),
                  in_specs=[pl.BlockSpec((tm, D), lambda i, j: (i, 0)),
                            pl.BlockSpec((256, D), lambda i, j: (_src_block(j), 0))],
                  out_specs=pl.BlockSpec((tm, 256), lambda i, j: (i, j)),
                  out_shape=jax.ShapeDtypeStruct((tt, NCOL), F32), carry=carry)(h, wt)


def _mm_dh(dp, wt, tm, carry=None):
    tt = dp.shape[0]
    per, ng = 3, NT_IN // 3

    def body(d_ref, w0, w1, w2, o_ref, acc):
        kk = pl.program_id(1)

        @pl.when(kk == 0)
        def _():
            acc[...] = jnp.zeros_like(acc)

        acc[...] += (_bdot(d_ref[:, 0:256], w0[...]) + _bdot(d_ref[:, 256:512], w1[...])
                     + _bdot(d_ref[:, 512:768], w2[...]))

        @pl.when(kk == ng - 1)
        def _():
            o_ref[...] = acc[...]

    wspecs = [pl.BlockSpec((256, D), functools.partial(lambda t, i, kk: (_src_block(per * kk + t), 0), t))
              for t in range(per)]
    return _pcall(body, name="mm_dh", grid=(tt // tm, ng),
                  in_specs=[pl.BlockSpec((tm, per * 256), lambda i, kk: (i, kk))] + wspecs,
                  out_specs=pl.BlockSpec((tm, D), lambda i, kk: (i, 0)),
                  out_shape=jax.ShapeDtypeStruct((tt, D), F32), scratch=[pltpu.VMEM((tm, D), F32)],
                  carry=carry)(dp, wt, wt, wt)


def _mm_gin(dp, h, tk):
    tt = dp.shape[0]
    nk = tt // tk

    def body(d_ref, h_ref, o_ref, acc):
        kk = pl.program_id(1)

        @pl.when(kk == 0)
        def _():
            acc[...] = jnp.zeros_like(acc)

        acc[...] += _bdot(d_ref[...], h_ref[...], TN)

        @pl.when(kk == nk - 1)
        def _():
            o_ref[...] = acc[...].astype(BF16)

    return _pcall(body, name="mm_gin", grid=(NT_IN, nk),
                  in_specs=[pl.BlockSpec((tk, 256), lambda j, kk: (kk, j)),
                            pl.BlockSpec((tk, D), lambda j, kk: (kk, 0))],
                  out_specs=pl.BlockSpec((256, D), lambda j, kk: (_src_block(j), 0)),
                  out_shape=jax.ShapeDtypeStruct((NCOL, D), BF16), scratch=[pltpu.VMEM((256, D), F32)])(dp, h)


def _tok_specs():
    assert L == TM
    return [_full((TM, D)), pl.BlockSpec((TM, D), lambda i: (jnp.maximum(i - 1, 0), 0))]


def _mod1(ctx, x, nw, ss):
    rows = L + x.shape[0]

    def body(c_ref, x_ref, nw_ref, ss_ref, h_ref):
        t = jnp.where(pl.program_id(0) == 0, c_ref[...], x_ref[...])
        r = lax.rsqrt(jnp.mean(t * t, axis=-1, keepdims=True) + EPS)
        s = ss_ref[0]
        h_ref[...] = ((t * r * nw_ref[...]) * (1.0 + s[1:2]) + s[0:1]).astype(BF16)

    return _pcall(body, name="mod1", grid=(rows // TM,),
                  in_specs=_tok_specs() + [_full((1, D)),
                                           pl.BlockSpec((1, 2, D), lambda i: (jnp.minimum(i, 1), 0, 0))],
                  out_specs=pl.BlockSpec((TM, D), lambda i: (i, 0)),
                  out_shape=jax.ShapeDtypeStruct((rows, D), BF16))(ctx, x, nw, ss)


def _norm_bwd_rows(x, dh, nw, scale):
    r = lax.rsqrt(jnp.mean(x * x, axis=-1, keepdims=True) + EPS)
    xh = x * r
    dxh = dh * ((1.0 + scale) * nw)
    dx = r * (dxh - xh * jnp.mean(dxh * xh, axis=-1, keepdims=True))
    return dx, xh


def _out_proj_mod2(mixed, w_o, x, g1, nw2, ss2):
    s_len = x.shape[0]
    tm = 512

    def body(m_ref, w_ref, x_ref, g_ref, nw_ref, ss_ref, ao_ref, x1_ref, h_ref):
        ao = _bdot(m_ref[...], w_ref[...])
        ao_ref[...] = ao.astype(BF16)
        x1 = x_ref[...] + g_ref[...] * ao
        x1_ref[...] = x1
        r = lax.rsqrt(jnp.mean(x1 * x1, axis=-1, keepdims=True) + EPS)
        s = ss_ref[0]
        h_ref[...] = ((x1 * r * nw_ref[...]) * (1.0 + s[1:2]) + s[0:1]).astype(BF16)

    row = pl.BlockSpec((tm, D), lambda i: (i, 0))
    f = jax.ShapeDtypeStruct((s_len, D), F32)
    return _pcall(body, name="out_proj_mod2", grid=(s_len // tm,),
                  in_specs=[row, _full((D, D)), row, _full((1, D)), _full((1, D)), _full((1, 2, D))],
                  out_specs=[row, row, row],
                  out_shape=[jax.ShapeDtypeStruct((s_len, D), BF16), f,
                             jax.ShapeDtypeStruct((s_len, D), BF16)])(mixed, w_o, x, g1, nw2, ss2)


TS = 1024


def _acc_call(body, *, name, grid, in_specs, out_specs, out_shape, acc_shapes, args, carry=None):
    return _pcall(body, name=name, grid=grid, in_specs=in_specs, out_specs=out_specs, out_shape=out_shape,
                  scratch=[pltpu.VMEM(s, F32) for s in acc_shapes], carry=carry)(*args)


def _ffn_up(h2, g4, u4, carry=None):
    s_len = h2.shape[0]
    ns = g4.shape[1]

    def body(h_ref, g_ref, u_ref, a_ref, b_ref, z_ref):
        h = h_ref[...]
        a = _bdot(h, g_ref[0], NT)
        b = _bdot(h, u_ref[0], NT)
        a_ref[0] = a.astype(BF16)
        b_ref[0] = b.astype(BF16)
        z_ref[0] = (a * _sig(a) * b).astype(BF16)

    w = pl.BlockSpec((1, ns, D), lambda i, j: (j, 0, 0))
    o = pl.BlockSpec((1, TS, ns), lambda i, j: (j, i, 0))
    f = jax.ShapeDtypeStruct((4, s_len, ns), BF16)
    return _pcall(body, name="ffn_up", grid=(s_len // TS, 4),
                  in_specs=[pl.BlockSpec((TS, D), lambda i, j: (i, 0)), w, w], out_specs=[o, o, o],
                  out_shape=[f, f, jax.ShapeDtypeStruct((4, s_len, ns), BF16)], carry=carry)(h2, g4, u4)


def _ffn_down_loss(z4, dn4, x1, g2, tgt):
    _, s_len, ns = z4.shape

    def body(z_ref, w_ref, x1_ref, g_ref, t_ref, sq_ref, dx2_ref, dyb_ref, dg_ref, acc):
        i, j = pl.program_id(0), pl.program_id(1)

        @pl.when((i == 0) & (j == 0))
        def _():
            sq_ref[...] = jnp.zeros_like(sq_ref)
            dg_ref[...] = jnp.zeros_like(dg_ref)

        @pl.when(j == 0)
        def _():
            acc[...] = jnp.zeros_like(acc)

        acc[...] += _bdot(z_ref[0], w_ref[0])

        @pl.when(j == 3)
        def _():
            y_ = acc[...]
            g = g_ref[...]
            e = x1_ref[...] + g * y_ - t_ref[...]
            sq_ref[...] += jnp.sum(e * e, axis=0, keepdims=True)
            dx2 = e * (1.0 / D)
            dx2_ref[...] = dx2
            dyb_ref[...] = (g * dx2).astype(BF16)
            dg_ref[...] += jnp.sum(dx2 * y_, axis=0, keepdims=True)

    row = pl.BlockSpec((TS, D), lambda i, j: (i, 0))
    vec = _full((1, D))
    return _acc_call(body, name="ffn_down_loss", grid=(s_len // TS, 4),
                     in_specs=[pl.BlockSpec((1, TS, ns), lambda i, j: (j, i, 0)),
                               pl.BlockSpec((1, ns, D), lambda i, j: (j, 0, 0)), row, vec, row],
                     out_specs=[vec, row, row, vec],
                     out_shape=[jax.ShapeDtypeStruct((1, D), F32), jax.ShapeDtypeStruct((s_len, D), F32),
                                jax.ShapeDtypeStruct((s_len, D), BF16), jax.ShapeDtypeStruct((1, D), F32)],
                     acc_shapes=[(TS, D)], args=(z4, dn4, x1, g2, tgt))


def _ffn_dz(dyb, dn4, a4, b4):
    _, s_len, ns = a4.shape

    def body(dy_ref, w_ref, a_ref, b_ref, da_ref, db_ref):
        dz = _bdot(dy_ref[...], w_ref[0], NT)
        a = a_ref[0].astype(F32)
        s = _sig(a)
        da_ref[0] = (dz * b_ref[0].astype(F32) * (s * (1.0 + a * (1.0 - s)))).astype(BF16)
        db_ref[0] = (dz * (a * s)).astype(BF16)

    t = pl.BlockSpec((1, TS, ns), lambda i, j: (j, i, 0))
    o = jax.ShapeDtypeStruct((4, s_len, ns), BF16)
    return _pcall(body, name="ffn_dz", grid=(s_len // TS, 4),
                  in_specs=[pl.BlockSpec((TS, D), lambda i, j: (i, 0)),
                            pl.BlockSpec((1, ns, D), lambda i, j: (j, 0, 0)), t, t],
                  out_specs=[t, t], out_shape=[o, o])(dyb, dn4, a4, b4)


def _ffn_gdn(z4, dyb):
    _, s_len, ns = z4.shape
    nk = s_len // TS

    def body(z_ref, dy_ref, o_ref, acc):
        t = pl.program_id(1)

        @pl.when(t == 0)
        def _():
            acc[...] = jnp.zeros_like(acc)

        acc[...] += _bdot(z_ref[0], dy_ref[...], TN)

        @pl.when(t == nk - 1)
        def _():
            o_ref[0] = acc[...].astype(o_ref.dtype)

    return _acc_call(body, name="ffn_gdn", grid=(4, nk),
                     in_specs=[pl.BlockSpec((1, TS, ns), lambda j, t: (j, t, 0)),
                               pl.BlockSpec((TS, D), lambda j, t: (t, 0))],
                     out_specs=pl.BlockSpec((1, ns, D), lambda j, t: (j, 0, 0)),
                     out_shape=jax.ShapeDtypeStruct((4, ns, D), BF16), acc_shapes=[(ns, D)], args=(z4, dyb))


def _ffn_dh2(da4, db4, g4, u4, carry=None):
    _, s_len, ns = da4.shape

    def body(da_ref, db_ref, g_ref, u_ref, o_ref, acc):
        j = pl.program_id(1)

        @pl.when(j == 0)
        def _():
            acc[...] = jnp.zeros_like(acc)

        acc[...] += _bdot(da_ref[0], g_ref[0]) + _bdot(db_ref[0], u_ref[0])

        @pl.when(j == 3)
        def _():
            o_ref[...] = acc[...]

    t = pl.BlockSpec((1, TS, ns), lambda i, j: (j, i, 0))
    w = pl.BlockSpec((1, ns, D), lambda i, j: (j, 0, 0))
    return _acc_call(body, name="ffn_dh2", grid=(s_len // TS, 4), in_specs=[t, t, w, w],
                     out_specs=pl.BlockSpec((TS, D), lambda i, j: (i, 0)),
                     out_shape=jax.ShapeDtypeStruct((s_len, D), F32), acc_shapes=[(TS, D)],
                     args=(da4, db4, g4, u4), carry=carry)


def _ffn_ggu(h2, da4, db4, carry=None):
    _, s_len, ns = da4.shape
    nk = s_len // TS

    def body(h_ref, da_ref, db_ref, gg_ref, gu_ref, acc_g, acc_u):
        t = pl.program_id(1)

        @pl.when(t == 0)
        def _():
            acc_g[...] = jnp.zeros_like(acc_g)
            acc_u[...] = jnp.zeros_like(acc_u)

        h = h_ref[...]
        acc_g[...] += _bdot(da_ref[0], h, TN)
        acc_u[...] += _bdot(db_ref[0], h, TN)

        @pl.when(t == nk - 1)
        def _():
            gg_ref[0] = acc_g[...].astype(BF16)
            gu_ref[0] = acc_u[...].astype(BF16)

    d = pl.BlockSpec((1, TS, ns), lambda j, t: (j, t, 0))
    o = pl.BlockSpec((1, ns, D), lambda j, t: (j, 0, 0))
    f = jax.ShapeDtypeStruct((4, ns, D), BF16)
    return _acc_call(body, name="ffn_ggu", grid=(4, nk),
                     in_specs=[pl.BlockSpec((TS, D), lambda j, t: (t, 0)), d, d], out_specs=[o, o],
                     out_shape=[f, f], acc_shapes=[(ns, D), (ns, D)], args=(h2, da4, db4), carry=carry)


def _mod2_bwd(x1, dh2, dx2, ao, nw2, ss2, g1):
    s_len = x1.shape[0]

    def body(x1_ref, dh_ref, dx2_ref, ao_ref, nw_ref, ss_ref, g_ref,
             dx1_ref, da_ref, dss_ref, dnw_ref, dg_ref):
        i = pl.program_id(0)

        @pl.when(i == 0)
        def _():
            dss_ref[...] = jnp.zeros_like(dss_ref)
            dnw_ref[...] = jnp.zeros_like(dnw_ref)
            dg_ref[...] = jnp.zeros_like(dg_ref)

        dh = dh_ref[...]
        nw = nw_ref[...]
        scale = ss_ref[0][1:2]
        dxn, xh = _norm_bwd_rows(x1_ref[...], dh, nw, scale)
        dx1 = dx2_ref[...] + dxn
        dx1_ref[...] = dx1
        da_ref[...] = (g_ref[...] * dx1).astype(BF16)
        dg_ref[...] += jnp.sum(dx1 * ao_ref[...].astype(F32), axis=0, keepdims=True)
        dsh = jnp.sum(dh, axis=0, keepdims=True)
        dsc = jnp.sum(dh * xh * nw, axis=0, keepdims=True)
        dss_ref[...] += jnp.concatenate([dsh, dsc], axis=0)
        dnw_ref[...] += jnp.sum(dh * xh * (1.0 + scale), axis=0, keepdims=True)

    row = pl.BlockSpec((TM, D), lambda i: (i, 0))
    vec = _full((1, D))
    return _pcall(body, name="mod2_bwd", grid=(s_len // TM,),
                  in_specs=[row, row, row, row, vec, _full((1, 2, D)), vec],
                  out_specs=[row, row, _full((2, D)), vec, vec],
                  out_shape=[jax.ShapeDtypeStruct((s_len, D), F32), jax.ShapeDtypeStruct((s_len, D), BF16),
                             jax.ShapeDtypeStruct((2, D), F32), jax.ShapeDtypeStruct((1, D), F32),
                             jax.ShapeDtypeStruct((1, D), F32)])(x1, dh2, dx2, ao, nw2, ss2, g1)


def _mod1_bwd(ctx, x, dh, dx1, nw1, ss1, carry=None):
    s_len = dx1.shape[0]
    tt = L + s_len

    def body(c_ref, x_ref, dh_ref, dx1_ref, nw_ref, ss_ref, dx_ref, dss_ref, dnw_ref):
        i = pl.program_id(0)
        tok = jnp.where(i == 0, c_ref[...], x_ref[...])

        @pl.when(i == 0)
        def _():
            dnw_ref[...] = jnp.zeros_like(dnw_ref)

        @pl.when(i <= 1)
        def _():
            dss_ref[...] = jnp.zeros_like(dss_ref)

        dh_ = dh_ref[...]
        nw = nw_ref[...]
        scale = ss_ref[0][1:2]
        dxn, xh = _norm_bwd_rows(tok, dh_, nw, scale)

        @pl.when(i >= 1)
        def _():
            dx_ref[...] = dx1_ref[...] + dxn

        dsh = jnp.sum(dh_, axis=0, keepdims=True)
        dsc = jnp.sum(dh_ * xh * nw, axis=0, keepdims=True)
        dss_ref[...] += jnp.concatenate([dsh, dsc], axis=0)[None]
        dnw_ref[...] += jnp.sum(dh_ * xh * (1.0 + scale), axis=0, keepdims=True)

    row = pl.BlockSpec((TM, D), lambda i: (i, 0))
    lat = pl.BlockSpec((TM, D), lambda i: (jnp.maximum(i - 1, 0), 0))
    sel = pl.BlockSpec((1, 2, D), lambda i: (jnp.minimum(i, 1), 0, 0))
    return _pcall(body, name="mod1_bwd", grid=(tt // TM,),
                  in_specs=_tok_specs() + [row, lat, _full((1, D)), sel],
                  out_specs=[lat, sel, _full((1, D))],
                  out_shape=[jax.ShapeDtypeStruct((s_len, D), F32), jax.ShapeDtypeStruct((2, 2, D), F32),
                             jax.ShapeDtypeStruct((1, D), F32)], carry=carry)(ctx, x, dh, dx1, nw1, ss1)


def _rows(c):
    return slice(c * CH, (c + 1) * CH)


def _chunk_masks(rev, transpose=False):
    r = lax.broadcasted_iota(jnp.int32, (TM, TM), 0)
    c = lax.broadcasted_iota(jnp.int32, (TM, TM), 1)
    same = (r // CH) == (c // CH)
    before = (c >= r) if (rev != transpose) else (c <= r)
    return same & before, same


def _chunk_scan(x, rev, transpose=False):
    r = lax.broadcasted_iota(jnp.int32, (CH, CH), 0)
    c = lax.broadcasted_iota(jnp.int32, (CH, CH), 1)
    tri = ((c >= r) if (rev != transpose) else (c <= r)).astype(F32)
    return jnp.concatenate([_dot(tri, x[_rows(ch)], prec=HI) for ch in range(x.shape[0] // CH)], axis=0)


def _chunk_total(x):
    return jnp.concatenate([jnp.broadcast_to(jnp.sum(x[_rows(ch)], axis=0, keepdims=True), (CH, x.shape[1]))
                            for ch in range(x.shape[0] // CH)], axis=0)


def _hgrn_gate(fl, qraw, lg):
    lb = 1.0 / (1.0 + jnp.exp(lg[1:2] - lg[0:1]))
    sg = _sig(fl)
    f = lb + (1.0 - lb) * sg
    q = qraw * _sig(qraw) * (HGD ** -0.5)
    return lb, sg, f, q


def _hgrn_fwd(p, lg, *, rev, carry=None, readout=None):
    tt = p.shape[0]
    nt = tt // TM
    ncht = TM // CH
    d = 1 if rev else 0

    def tile_of(s):
        return jnp.where(s == 0, 0, nt - s) if rev else s

    def body(*refs):
        if readout is None:
            f_ref, inp_ref, q_ref, lg_ref, o_ref, st_ref, state = refs
        else:
            f_ref, inp_ref, q_ref, lg_ref, oo_ref, g_ref, hw_ref, o_ref, st_ref, y_ref, state = refs
        s = pl.program_id(0)

        @pl.when(s == 0)
        def _():
            state[...] = jnp.zeros_like(state)

        _, _, f, q = _hgrn_gate(f_ref[...], q_ref[...], lg_ref[0])
        lf = jnp.log(f)
        causal, _ = _chunk_masks(rev)
        cum = _chunk_scan(lf, rev)
        tot = _chunk_total(lf)
        qd = (q * jnp.exp(cum)).astype(BF16)
        kd = ((1.0 - f) * jnp.exp(-cum)).astype(BF16)
        ke = ((1.0 - f) * jnp.exp(tot - cum)).astype(BF16)
        et = jnp.exp(tot)
        v = inp_ref[...].astype(BF16)
        order = range(ncht - 1, -1, -1) if rev else range(ncht)
        outs = []
        for h in range(4):
            sl = slice(h * HGD, (h + 1) * HGD)
            qd_, kd_, ke_, v_ = qd[:, sl], kd[:, sl], ke[:, sl], v[:, sl]
            pm = jnp.where(causal, _dot(qd_, kd_, NT), 0.0).astype(BF16)
            o_h = _dot(pm, v_)
            upd = [_dot(v_[_rows(c)], ke_[_rows(c)], TN) for c in range(ncht)]
            st = state[h]
            for c in order:
                st_ref[c, h] = st
                st = st * et[c * CH:c * CH + 1, sl] + upd[c]
            state[h] = st
            inter = [_dot(qd_[_rows(c)], st_ref[c, h].astype(BF16), NT) for c in range(ncht)]
            outs.append(o_h + jnp.concatenate(inter, axis=0))
        o_tile = jnp.concatenate(outs, axis=1)
        o_ref[...] = o_tile
        if readout is not None:
            @pl.when(tile_of(s) >= 1)
            def _():
                g = g_ref[...]
                y_ref[...] = (_head_rms(oo_ref[...] + o_tile, None, 4) * hw_ref[...] * (g * _sig(g))).astype(BF16)

    def col(cb):
        return pl.BlockSpec((TM, HGW), lambda s: (tile_of(s), cb))

    in_specs = [col(C_FB if rev else C_FF), col(C_INP), col(C_QHG), pl.BlockSpec((1, 2, HGW), lambda s: (d, 0, 0))]
    out_specs = [col(0), pl.BlockSpec((ncht, 4, HGD, HGD), lambda s: (tile_of(s), 0, 0, 0))]
    out_shape = [jax.ShapeDtypeStruct((tt, HGW), F32), jax.ShapeDtypeStruct((nt * ncht, 4, HGD, HGD), F32)]
    args = [p, p, p, lg]
    if readout is not None:
        in_specs += [col(0), col(C_GHG), _full((1, HGW))]
        args += [readout[0], p, readout[1]]
        assert rev
        out_specs.append(pl.BlockSpec((TM, HGW), lambda s: (jnp.where(s == 0, nt - 2, tile_of(s) - 1), 0)))
        out_shape.append(jax.ShapeDtypeStruct((tt - L, HGW), BF16))
    return _pcall(body, name="hgrn_fwd_rev" if rev else "hgrn_fwd", grid=(nt,), in_specs=in_specs,
                  out_specs=out_specs, out_shape=out_shape, scratch=[pltpu.VMEM((4, HGD, HGD), F32)],
                  carry=carry)(*args)


def _hgrn_bwd(p, lg, do, st, dp, prev, *, rev, carry=None):
    tt = p.shape[0]
    nt = tt // TM
    ncht = TM // CH
    d = 1 if rev else 0
    second = prev is not None

    def tile_of(s):
        return jnp.where(s == nt - 1, 0, s + 1) if rev else nt - 1 - s

    def body(*refs):
        if second:
            (f_ref, inp_ref, q_ref, lg_ref, do_ref, st_ref, dvp_ref, dqp_ref, _dp_in,
             dp_ref, dlg_ref, dstate) = refs
        else:
            (f_ref, inp_ref, q_ref, lg_ref, do_ref, st_ref, _dp_in,
             dp_ref, dv_ref, dq_ref, dlg_ref, dstate) = refs
        s = pl.program_id(0)
        tile = tile_of(s)

        @pl.when(s == 0)
        def _():
            dstate[...] = jnp.zeros_like(dstate)
            dlg_ref[...] = jnp.zeros_like(dlg_ref)

        qraw = q_ref[...]
        lb, sg, f, q = _hgrn_gate(f_ref[...], qraw, lg_ref[0])
        lf = jnp.log(f)
        causal, _ = _chunk_masks(rev)
        causal_t, _ = _chunk_masks(rev, transpose=True)
        cum = _chunk_scan(lf, rev)
        tot = _chunk_total(lf)
        ea, eb, ee, et = jnp.exp(cum), jnp.exp(-cum), jnp.exp(tot - cum), jnp.exp(tot)
        qdf, kdf, kef = q * ea, (1.0 - f) * eb, (1.0 - f) * ee
        qd, kd, ke = qdf.astype(BF16), kdf.astype(BF16), kef.astype(BF16)
        v = inp_ref[...].astype(BF16)
        dob = jnp.where(tile == 0, 0.0, do_ref[...]).astype(BF16)
        order = range(ncht) if rev else range(ncht - 1, -1, -1)
        dq_l, dk_l, dv_l, dcum_l, dtot_l = [], [], [], [], []
        for h in range(4):
            sl = slice(h * HGD, (h + 1) * HGD)
            qd_, kd_, ke_, v_, do_ = qd[:, sl], kd[:, sl], ke[:, sl], v[:, sl], dob[:, sl]
            pmt = jnp.where(causal_t, _dot(kd_, qd_, NT), 0.0).astype(BF16)
            dpm = jnp.where(causal, _dot(do_, v_, NT), 0.0).astype(BF16)
            dpmt = jnp.where(causal_t, _dot(v_, do_, NT), 0.0).astype(BF16)
            dv = _dot(pmt, do_)
            dqd = _dot(dpm, kd_)
            dkd = _dot(dpmt, qd_)
            upd = [_dot(do_[_rows(c)], qd_[_rows(c)], TN) for c in range(ncht)]
            ds = dstate[h]
            ds1 = [None] * ncht
            for c in order:
                ds1[c] = ds
                ds = ds * et[c * CH:c * CH + 1, sl] + upd[c]
            dstate[h] = ds
            dke_c, dv_c, dqd_c, dtot_c = [], [], [], []
            for c in range(ncht):
                st0 = st_ref[c, h]
                dsb = ds1[c].astype(BF16)
                dke_ = _dot(v_[_rows(c)], dsb)
                dke_c.append(dke_)
                dv_c.append(_dot(ke_[_rows(c)], dsb, NT))
                dqd_c.append(_dot(do_[_rows(c)], st0.astype(BF16)))
                dt = (jnp.sum(ds1[c] * st0, axis=0, keepdims=True) * et[c * CH:c * CH + 1, sl]
                      + jnp.sum(dke_ * kef[_rows(c), sl], axis=0, keepdims=True))
                dtot_c.append(jnp.broadcast_to(dt, (CH, HGD)))
            dke = jnp.concatenate(dke_c, axis=0)
            dqd = dqd + jnp.concatenate(dqd_c, axis=0)
            dv_l.append(dv + jnp.concatenate(dv_c, axis=0))
            dtot_l.append(jnp.concatenate(dtot_c, axis=0))
            dq_l.append(dqd * ea[:, sl])
            dk_l.append(dkd * eb[:, sl] + dke * ee[:, sl])
            dcum_l.append(dqd * qdf[:, sl] - dkd * kdf[:, sl] - dke * kef[:, sl])
        dcum = jnp.concatenate(dcum_l, axis=1)
        dlf = _chunk_scan(dcum, rev, transpose=True) + jnp.concatenate(dtot_l, axis=1)
        dq_t = jnp.concatenate(dq_l, axis=1)
        dv_t = jnp.concatenate(dv_l, axis=1)

        df = dlf / f - jnp.concatenate(dk_l, axis=1)
        dfl = df * (1.0 - lb) * sg * (1.0 - sg)
        dlb = jnp.sum(df * (1.0 - sg), axis=0, keepdims=True)
        dl0 = dlb * lb * (1.0 - lb)
        dlg_ref[...] += jnp.concatenate([dl0, -dl0], axis=0)[None]
        if second:
            sq = _sig(qraw)
            dqr = (dqp_ref[...] + dq_t) * (HGD ** -0.5) * (sq * (1.0 + qraw * (1.0 - sq)))
            dp_ref[...] = jnp.concatenate([dfl, dvp_ref[...] + dv_t, dqr], axis=1).astype(BF16)
        else:
            dp_ref[...] = dfl.astype(BF16)
            dv_ref[...] = dv_t
            dq_ref[...] = dq_t

    def col(cb):
        return pl.BlockSpec((TM, HGW), lambda s: (tile_of(s), cb))

    tok = pl.BlockSpec((TM, HGW), lambda s: (tile_of(s), 0))
    in_specs = [col(C_FB if rev else C_FF), col(C_INP), col(C_QHG),
                pl.BlockSpec((1, 2, HGW), lambda s: (d, 0, 0)),
                pl.BlockSpec((TM, HGW), lambda s: (jnp.maximum(tile_of(s) - 1, 0), 0)),
                pl.BlockSpec((ncht, 4, HGD, HGD), lambda s: (tile_of(s), 0, 0, 0))]
    args = [p, p, p, lg, do, st]
    dlg_spec = _full((1, 2, HGW))
    dlg_shape = jax.ShapeDtypeStruct((1, 2, HGW), F32)
    if second:
        in_specs += [tok, tok]
        args += [prev[0], prev[1]]
        out_specs = [pl.BlockSpec((TM, 3 * HGW), lambda s: (tile_of(s), 0)), dlg_spec]
        out_shape = [jax.ShapeDtypeStruct(dp.shape, BF16), dlg_shape]
    else:
        out_specs = [pl.BlockSpec((TM, HGW), lambda s: (tile_of(s), C_FB if rev else C_FF)), tok, tok, dlg_spec]
        out_shape = [jax.ShapeDtypeStruct(dp.shape, BF16), jax.ShapeDtypeStruct((tt, HGW), F32),
                     jax.ShapeDtypeStruct((tt, HGW), F32), dlg_shape]
    in_specs.append(ANY)
    args.append(dp)
    return _pcall(body, name="hgrn_bwd_rev" if rev else "hgrn_bwd", grid=(nt,),
                  in_specs=in_specs, out_specs=out_specs, out_shape=out_shape,
                  scratch=[pltpu.VMEM((4, HGD, HGD), F32)],
                  aliases={len(args) - 1: 0}, carry=carry)(*args)


def _head_rms(o, w, nheads):
    outs = []
    for h in range(nheads):
        oh = o[:, h * HGD:(h + 1) * HGD]
        outs.append(oh * lax.rsqrt(jnp.mean(oh * oh, axis=-1, keepdims=True) + EPS))
    return jnp.concatenate(outs, axis=1)


def _readout_bwd(o0, o1, p, hw4, dy, dp, carry=None):
    tt = o0.shape[0]
    s_len = tt - L

    def body(o0_ref, o1_ref, g_ref, w_ref, dy_ref, _dp_in, dp_ref, do_ref, dw_ref):
        i = pl.program_id(0)

        @pl.when(i == 0)
        def _():
            dw_ref[...] = jnp.zeros_like(dw_ref)
            dp_ref[...] = jnp.zeros_like(dp_ref)

        @pl.when(i >= 1)
        def _():
            o = o0_ref[...] + o1_ref[...]
            g = g_ref[...]
            w = w_ref[...]
            sg = _sig(g)
            dy_ = dy_ref[...]
            dsw = dy_ * (g * sg)
            outs, xhs = [], []
            for h in range(4):
                sl = slice(h * HGD, (h + 1) * HGD)
                oh = o[:, sl]
                r = lax.rsqrt(jnp.mean(oh * oh, axis=-1, keepdims=True) + EPS)
                xh = oh * r
                dxh = dsw[:, sl] * w[:, sl]
                outs.append(r * (dxh - xh * jnp.mean(dxh * xh, axis=-1, keepdims=True)))
                xhs.append(xh)
            xh = jnp.concatenate(xhs, axis=1)
            do_ref[...] = jnp.concatenate(outs, axis=1)
            dp_ref[...] = (dy_ * xh * w * (sg * (1.0 + g * (1.0 - sg)))).astype(BF16)
            dw_ref[...] += jnp.sum(dsw * xh, axis=0, keepdims=True)

    tok = pl.BlockSpec((TM, HGW), lambda i: (i, 0))
    lat = pl.BlockSpec((TM, HGW), lambda i: (jnp.maximum(i - 1, 0), 0))
    return _pcall(body, name="readout_bwd", grid=(tt // TM,),
                  in_specs=[tok, tok, pl.BlockSpec((TM, HGW), lambda i: (i, C_GHG)), _full((1, HGW)), lat, ANY],
                  out_specs=[pl.BlockSpec((TM, HGW), lambda i: (i, C_GHG)), lat, _full((1, HGW))],
                  out_shape=[jax.ShapeDtypeStruct(dp.shape, BF16), jax.ShapeDtypeStruct((s_len, HGW), F32),
                             jax.ShapeDtypeStruct((1, HGW), F32)],
                  aliases={5: 0}, carry=carry)(o0, o1, p, hw4, dy, dp)


def _rope_tables(s_len):
    t = np.arange(s_len)
    inv = ROPE_THETA ** (-np.arange(0, 32, 2, dtype=np.float64) / 32)
    def half(pos):
        ang = pos[:, None].astype(np.float64) * inv[None, :]
        return (np.concatenate([np.cos(ang), np.cos(ang)], 1), np.concatenate([-np.sin(ang), np.sin(ang)], 1))
    cr, sr = half(t // GRID_W)
    cc, sc = half(t % GRID_W)
    cos = np.concatenate([cr, cc, cr, cc], 1)
    sin = np.concatenate([sr, sc, sr, sc], 1)
    cos = np.concatenate([np.ones((L, 128)), cos], 0)
    sin = np.concatenate([np.zeros((L, 128)), sin], 0)
    return jnp.asarray(cos, F32), jnp.asarray(sin, F32)


def _blockdiag(n, w):
    i = np.arange(n)
    return jnp.asarray((i[:, None] // w == i[None, :] // w) / float(w), F32)


def _dup_matrix():
    m = np.zeros((128, 512), np.float32)
    for g in range(2):
        for j in range(4):
            for dd in range(HDIM):
                m[64 * g + dd, 256 * g + 64 * j + dd] = 1.0
    return m


def _head_mean(x, blockdiag):
    return _dot(x, blockdiag, prec=lax.Precision.HIGH)


def _rot(x):
    n = x.shape[1]
    lane = lax.broadcasted_iota(jnp.int32, x.shape, 1)
    return jnp.where((lane % 32) < 16, pltpu.roll(x, n - 16, 1), pltpu.roll(x, 16, 1))


def _qk_prep(p, cos, sin, qnw8, knw2, bd512, bd128, dup):
    tt = p.shape[0]

    def body(q_ref, kv_ref, cos_ref, sin_ref, qw_ref, kw_ref, b5_ref, b1_ref, dup_ref,
             qr_ref, k4_ref, v4_ref):
        cos_, sin_ = cos_ref[...], sin_ref[...]
        q = q_ref[...]
        qn = q * lax.rsqrt(_head_mean(q * q, b5_ref[...]) + EPS) * qw_ref[...]
        cos4 = jnp.concatenate([cos_] * 4, axis=1)
        sin4 = jnp.concatenate([sin_] * 4, axis=1)
        qr_ref[...] = ((qn * cos4 + _rot(qn) * sin4) * (HDIM ** -0.5)).astype(BF16)
        kv = kv_ref[...]
        k, v = kv[:, :128], kv[:, 128:]
        kn = k * lax.rsqrt(_head_mean(k * k, b1_ref[...]) + EPS) * kw_ref[...]
        kr = kn * cos_ + _rot(kn) * sin_
        k4_ref[...] = _bdot(kr, dup_ref[...]).astype(BF16)
        v4_ref[...] = _bdot(v, dup_ref[...]).astype(BF16)

    row = lambda w, cb: pl.BlockSpec((TM, w), lambda i: (i, cb))
    out = jax.ShapeDtypeStruct((tt, ATW), BF16)
    return _pcall(body, name="qk_prep", grid=(tt // TM,),
                  in_specs=[row(ATW, C_QRAW), row(256, C_KV), row(128, 0), row(128, 0),
                            _full((1, ATW)), _full((1, 128)), _full((ATW, ATW)), _full((128, 128)),
                            _full((128, ATW))],
                  out_specs=[row(ATW, 0)] * 3, out_shape=[out] * 3)(
                      p, p, cos, sin, qnw8, knw2, bd512, bd128, dup)


def _attn_masks(i, nb):
    r = lax.broadcasted_iota(jnp.int32, (4 * BLK, 3 * BLK + L), 0) % BLK
    c = lax.broadcasted_iota(jnp.int32, (4 * BLK, 3 * BLK + L), 1)
    kpos = (i - 1) * BLK + c
    loc = (jnp.abs(c - BLK - r) <= BLK) & (kpos >= 0) & (kpos < nb * BLK)
    return loc | (c >= 3 * BLK)


def _stack_mask():
    r = lax.broadcasted_iota(jnp.int32, (4 * BLK, 256), 0)
    lane = lax.broadcasted_iota(jnp.int32, (4 * BLK, 256), 1)
    return (r // BLK) == (lane // HDIM)


def _stack_heads(xg, fill=0.0):
    x4 = jnp.concatenate([xg] * 4, axis=0)
    return jnp.where(_stack_mask(), x4, jnp.full_like(x4, fill))


def _unstack_heads(x4):
    out = jnp.where(_lane_mask(0), x4[0:BLK], 0.0)
    for j in range(1, 4):
        out = out + jnp.where(_lane_mask(j), x4[j * BLK:(j + 1) * BLK], 0.0)
    return out


def _per_head_rows(vals):
    return jnp.concatenate([jnp.broadcast_to(v, (BLK, 1)) for v in vals], axis=0)


def _lane_mask(j):
    lane = lax.broadcasted_iota(jnp.int32, (1, 256), 1)
    return (lane // HDIM) == j


def _attn_specs(nb):
    blk = lambda off: pl.BlockSpec((BLK, ATW), lambda i: (jnp.clip(i + off, 0, nb - 1) + 2, 0))
    ctx = pl.BlockSpec((L, ATW), lambda i: (0, 0))
    return blk, ctx


def _attn_fwd(qr, k4, v4, sinks, carry=None):
    tt = qr.shape[0]
    s_len = tt - L
    nb = s_len // BLK

    def body(sk_ref, q_ref, kp, ko, kn, kc, vp, vo, vn, vc, y_ref, lse_ref):
        i = pl.program_id(0)
        valid = _attn_masks(i, nb)
        q = q_ref[...]
        ys, lses = [], []
        for g in range(2):
            gs = slice(256 * g, 256 * g + 256)
            kcat = jnp.concatenate([kp[:, gs], ko[:, gs], kn[:, gs], kc[:, gs]], axis=0)
            vcat = jnp.concatenate([vp[:, gs], vo[:, gs], vn[:, gs], vc[:, gs]], axis=0)
            sink4 = _per_head_rows([sk_ref[4 * g + j] for j in range(4)])
            q4 = _stack_heads(q[:, gs])
            o_parts, l_parts = [], []
            for hp in range(2):
                rows = slice(2 * BLK * hp, 2 * BLK * (hp + 1))
                sink = sink4[rows]
                s = jnp.where(valid[rows], _dot(q4[rows], kcat, NT), -1e30)
                m = jnp.maximum(jnp.max(s, axis=-1, keepdims=True), sink)
                e = jnp.exp(s - m)
                den = jnp.sum(e, axis=-1, keepdims=True) + jnp.exp(sink - m)
                o_parts.append(_bdot(e * (1.0 / den), vcat))
                l_parts.append(jnp.broadcast_to(m + jnp.log(den), (2 * BLK, 256)))
            ys.append(_unstack_heads(jnp.concatenate(o_parts, axis=0)))
            lses.append(_unstack_heads(jnp.concatenate(l_parts, axis=0)))
        y_ref[...] = jnp.concatenate(ys, axis=1).astype(BF16)
        lse_ref[...] = jnp.concatenate(lses, axis=1)

    blk, ctx = _attn_specs(nb)
    out = pl.BlockSpec((BLK, ATW), lambda i: (i, 0))
    return _pcall(body, name="attn_fwd", grid=(nb,),
                  in_specs=[pl.BlockSpec(memory_space=pltpu.SMEM), blk(0),
                            blk(-1), blk(0), blk(1), ctx, blk(-1), blk(0), blk(1), ctx],
                  out_specs=[out, out],
                  out_shape=[jax.ShapeDtypeStruct((s_len, ATW), BF16),
                             jax.ShapeDtypeStruct((s_len, ATW), F32)], carry=carry)(
                      sinks, qr, k4, k4, k4, k4, v4, v4, v4, v4)


def _attn_bwd(qr, k4, v4, sinks, y, lse, dy, carry=None):
    tt = qr.shape[0]
    s_len = tt - L
    nb = s_len // BLK

    def body(sk_ref, q_ref, kp, ko, kn, kc, vp, vo, vn, vc, y_ref, lse_ref, dy_ref,
             dq_ref, dkw_ref, dvw_ref, dkc_ref, dvc_ref, dsk_ref):
        i = pl.program_id(0)

        @pl.when(i == 0)
        def _():
            dkc_ref[...] = jnp.zeros_like(dkc_ref)
            dvc_ref[...] = jnp.zeros_like(dvc_ref)
            dsk_ref[...] = jnp.zeros_like(dsk_ref)

        valid = _attn_masks(i, nb)
        q = q_ref[...]
        dy_ = dy_ref[...]
        dly = dy_ * y_ref[...].astype(F32)
        lse_ = lse_ref[...]
        dqs = []
        for g in range(2):
            gs = slice(256 * g, 256 * g + 256)
            kcat = jnp.concatenate([kp[:, gs], ko[:, gs], kn[:, gs], kc[:, gs]], axis=0)
            vcat = jnp.concatenate([vp[:, gs], vo[:, gs], vn[:, gs], vc[:, gs]], axis=0)
            q4 = _stack_heads(q[:, gs])
            dy4 = _stack_heads(dy_[:, gs]).astype(BF16)
            lse4 = jnp.max(_stack_heads(lse_[:, gs], fill=-1e30), axis=-1, keepdims=True)
            delta = jnp.sum(_stack_heads(dly[:, gs]), axis=-1, keepdims=True)
            sink = _per_head_rows([sk_ref[4 * g + j] for j in range(4)])
            pr = jnp.where(valid, jnp.exp(_dot(q4, kcat, NT) - lse4), 0.0)
            dsb = (pr * (_dot(dy4, vcat, NT) - delta)).astype(BF16)
            dsink = jnp.exp(sink - lse4) * delta
            for j in range(4):
                dsk_ref[4 * g + j:4 * g + j + 1, :] += jnp.broadcast_to(
                    -jnp.sum(dsink[j * BLK:(j + 1) * BLK], axis=0, keepdims=True), (1, 128))
            dqs.append(_unstack_heads(_dot(dsb, kcat)))
            dkg = _dot(dsb, q4, TN)
            dvg = _dot(pr.astype(BF16), dy4, TN)
            dkw_ref[0, :, gs] = dkg[:3 * BLK]
            dvw_ref[0, :, gs] = dvg[:3 * BLK]
            dkc_ref[:, gs] += dkg[3 * BLK:]
            dvc_ref[:, gs] += dvg[3 * BLK:]
        dq_ref[...] = jnp.concatenate(dqs, axis=1)

    blk, ctx = _attn_specs(nb)
    out = pl.BlockSpec((BLK, ATW), lambda i: (i, 0))
    win = pl.BlockSpec((1, 3 * BLK, ATW), lambda i: (i, 0, 0))
    acc = _full((L, ATW))
    return _pcall(body, name="attn_bwd", grid=(nb,),
                  in_specs=[pl.BlockSpec(memory_space=pltpu.SMEM), blk(0),
                            blk(-1), blk(0), blk(1), ctx, blk(-1), blk(0), blk(1), ctx, out, out, out],
                  out_specs=[out, win, win, acc, acc, _full((8, 128))],
                  out_shape=[jax.ShapeDtypeStruct((s_len, ATW), F32),
                             jax.ShapeDtypeStruct((nb, 3 * BLK, ATW), F32),
                             jax.ShapeDtypeStruct((nb, 3 * BLK, ATW), F32),
                             jax.ShapeDtypeStruct((L, ATW), F32), jax.ShapeDtypeStruct((L, ATW), F32),
                             jax.ShapeDtypeStruct((8, 128), F32)], carry=carry)(
                      sinks, qr, k4, k4, k4, k4, v4, v4, v4, v4, y, lse, dy)


def _attn_post(p, cos, sin, qnw8, knw2, bd512, bd128, dupt, dq, dkw, dvw, dkc, dvc, dp, carry=None):
    tt = p.shape[0]
    s_len = tt - L
    nb = s_len // BLK
    nctx = L // BLK

    def body(q_ref, kv_ref, cos_ref, sin_ref, qw_ref, kw_ref, b5_ref, b1_ref, dupt_ref,
             dq_ref, kwp, kwo, kwn, vwp, vwo, vwn, dkc_ref, dvc_ref, _dp_in,
             dp_ref, dqw_ref, dkw_ref):
        t = pl.program_id(0)
        j = t - nctx

        @pl.when(t == 0)
        def _():
            dqw_ref[...] = jnp.zeros_like(dqw_ref)
            dkw_ref[...] = jnp.zeros_like(dkw_ref)

        is_lat = t >= nctx
        cos_, sin_ = cos_ref[...], sin_ref[...]
        has_p = is_lat & (j >= 1)
        has_n = is_lat & (j <= nb - 2)
        dk4 = (jnp.where(is_lat, kwo[0], dkc_ref[...]) + jnp.where(has_p, kwp[0], 0.0)
               + jnp.where(has_n, kwn[0], 0.0))
        dv4 = (jnp.where(is_lat, vwo[0], dvc_ref[...]) + jnp.where(has_p, vwp[0], 0.0)
               + jnp.where(has_n, vwn[0], 0.0))
        dkr = _dot(dk4, dupt_ref[...], prec=HI)
        dv = _dot(dv4, dupt_ref[...], prec=HI)
        kv = kv_ref[...]
        k = kv[:, :128]
        kw = kw_ref[...]
        rk = lax.rsqrt(_head_mean(k * k, b1_ref[...]) + EPS)
        xk = k * rk
        dkn = dkr * cos_ + _rot(dkr * sin_)
        dxk = dkn * kw
        dk = rk * (dxk - xk * _head_mean(dxk * xk, b1_ref[...]))
        dkw_ref[...] += jnp.sum(dkn * xk, axis=0, keepdims=True)
        q = q_ref[...]
        qw = qw_ref[...]
        rq = lax.rsqrt(_head_mean(q * q, b5_ref[...]) + EPS)
        xq = q * rq
        cos4 = jnp.concatenate([cos_] * 4, axis=1)
        sin4 = jnp.concatenate([sin_] * 4, axis=1)
        dqr = jnp.where(is_lat, dq_ref[...], 0.0) * (HDIM ** -0.5)
        dqn = dqr * cos4 + _rot(dqr * sin4)
        dxq = dqn * qw
        dqraw = rq * (dxq - xq * _head_mean(dxq * xq, b5_ref[...]))
        dqw_ref[...] += jnp.sum(dqn * xq, axis=0, keepdims=True)
        dp_ref[...] = jnp.concatenate([dqraw, dk, dv], axis=1).astype(BF16)

    row = lambda w, cb: pl.BlockSpec((BLK, w), lambda t: (t, cb))
    lat = pl.BlockSpec((BLK, ATW), lambda t: (jnp.maximum(t - nctx, 0), 0))

    def part(off):
        return pl.BlockSpec((1, BLK, ATW), lambda t: (jnp.clip(t - nctx + off, 0, nb - 1), 1 - off, 0))

    cacc = pl.BlockSpec((BLK, ATW), lambda t: (jnp.minimum(t, nctx - 1), 0))
    return _pcall(body, name="attn_post", grid=(tt // BLK,),
                  in_specs=[row(ATW, C_QRAW), row(256, C_KV), row(128, 0), row(128, 0),
                            _full((1, ATW)), _full((1, 128)), _full((ATW, ATW)), _full((128, 128)),
                            _full((ATW, 128)), lat, part(-1), part(0), part(1), part(-1), part(0), part(1),
                            cacc, cacc, ANY],
                  out_specs=[pl.BlockSpec((BLK, 768), lambda t: (t, C_QKV)), _full((1, ATW)), _full((1, 128))],
                  out_shape=[jax.ShapeDtypeStruct(dp.shape, BF16), jax.ShapeDtypeStruct((1, ATW), F32),
                             jax.ShapeDtypeStruct((1, 128), F32)],
                  aliases={18: 0}, carry=carry)(p, p, cos, sin, qnw8, knw2, bd512, bd128, dupt,
                                   dq, dkw, dkw, dkw, dvw, dvw, dvw, dkc, dvc, dp)


def _branch_merge(y_hg, y_at, bh4, ba4, p):
    s_len = y_hg.shape[0]

    def body(yh_ref, ya_ref, bh_ref, ba_ref, gh_ref, ga_ref, ah_ref, aa_ref, m_ref):
        yh, ya = yh_ref[...], ya_ref[...]
        ah = jnp.concatenate([_bdot(yh, bh_ref[j]) for j in range(4)], axis=1)
        aa = jnp.concatenate([_bdot(ya, ba_ref[j]) for j in range(4)], axis=1)
        ah_ref[...] = ah.astype(BF16)
        aa_ref[...] = aa.astype(BF16)
        m_ref[...] = (_sig(gh_ref[...]) * ah + _sig(ga_ref[...]) * aa).astype(BF16)

    row = pl.BlockSpec((TM, D), lambda i: (i, 0))
    y = pl.BlockSpec((TM, HGW), lambda i: (i, 0))
    f = jax.ShapeDtypeStruct((s_len, D), BF16)
    return _pcall(body, name="branch_merge", grid=(s_len // TM,),
                  in_specs=[y, y, _full(bh4.shape), _full(ba4.shape),
                            pl.BlockSpec((TM, D), lambda i: (i + 1, 2)), pl.BlockSpec((TM, D), lambda i: (i + 1, 3))],
                  out_specs=[row, row, row],
                  out_shape=[f, f, jax.ShapeDtypeStruct((s_len, D), BF16)])(y_hg, y_at, bh4, ba4, p, p)


def _branch_bwd(dmh, dma, bh4, ba4, y_hg, y_at):
    s_len = dmh.shape[0]
    nk = s_len // TS
    ns = D // 4

    def body(dh_ref, da_ref, bh_ref, ba_ref, yh_ref, ya_ref, dyh_ref, dya_ref, gh_ref, ga_ref, acc_h, acc_a):
        t = pl.program_id(0)

        @pl.when(t == 0)
        def _():
            acc_h[...] = jnp.zeros_like(acc_h)
            acc_a[...] = jnp.zeros_like(acc_a)

        for d_ref, w_ref, y_ref, dy_ref, acc in ((dh_ref, bh_ref, yh_ref, dyh_ref, acc_h),
                                                 (da_ref, ba_ref, ya_ref, dya_ref, acc_a)):
            y = y_ref[...]
            dy = jnp.zeros((TS, HGW), F32)
            for j in range(4):
                dj = d_ref[:, j * ns:(j + 1) * ns]
                dy = dy + _bdot(dj, w_ref[j], NT)
                acc[j] += _bdot(y, dj, TN)
            dy_ref[...] = dy

        @pl.when(t == nk - 1)
        def _():
            gh_ref[...] = acc_h[...].astype(BF16)
            ga_ref[...] = acc_a[...].astype(BF16)

    dm = pl.BlockSpec((TS, D), lambda t: (t, 0))
    y = pl.BlockSpec((TS, HGW), lambda t: (t, 0))
    w = _full(bh4.shape)
    fy = jax.ShapeDtypeStruct((s_len, HGW), F32)
    gw = jax.ShapeDtypeStruct(bh4.shape, BF16)
    return _pcall(body, name="branch_bwd", grid=(nk,), in_specs=[dm, dm, w, w, y, y],
                  out_specs=[y, y, w, w], out_shape=[fy, fy, gw, gw],
                  scratch=[pltpu.VMEM(bh4.shape, F32)] * 2)(dmh, dma, bh4, ba4, y_hg, y_at)


def _merge_bwd(dattn, w_o, mixed, ah, aa, p, carry=None):
    tt = p.shape[0]
    s_len = tt - L
    nt = tt // TM

    def body(da_ref, wo_ref, mx_ref, ah_ref, aa_ref, gh_ref, ga_ref, dp_ref, dmh_ref, dma_ref, go_ref, acc):
        i = pl.program_id(0)

        @pl.when(i == 0)
        def _():
            dp_ref[...] = jnp.zeros_like(dp_ref)
            acc[...] = jnp.zeros_like(acc)

        @pl.when(i >= 1)
        def _():
            da = da_ref[...]
            acc[...] += _bdot(mx_ref[...], da, TN)
            dm_ = _bdot(da, wo_ref[...], NT)
            sh, sa = _sig(gh_ref[...]), _sig(ga_ref[...])
            dp_ref[...] = jnp.concatenate([dm_ * ah_ref[...].astype(F32) * sh * (1.0 - sh),
                                           dm_ * aa_ref[...].astype(F32) * sa * (1.0 - sa)], axis=1).astype(BF16)
            dmh_ref[...] = (dm_ * sh).astype(BF16)
            dma_ref[...] = (dm_ * sa).astype(BF16)

        @pl.when(i == nt - 1)
        def _():
            go_ref[...] = acc[...].astype(BF16)

    lat = pl.BlockSpec((TM, D), lambda i: (jnp.maximum(i - 1, 0), 0))
    return _pcall(body, name="merge_bwd", grid=(nt,),
                  in_specs=[lat, _full((D, D)), lat, lat, lat, pl.BlockSpec((TM, D), lambda i: (i, 2)),
                            pl.BlockSpec((TM, D), lambda i: (i, 3))],
                  out_specs=[pl.BlockSpec((TM, 2 * D), lambda i: (i, C_GATES)), lat, lat, _full((D, D))],
                  out_shape=[jax.ShapeDtypeStruct((tt, NCOL), BF16), jax.ShapeDtypeStruct((s_len, D), BF16),
                             jax.ShapeDtypeStruct((s_len, D), BF16), jax.ShapeDtypeStruct((D, D), BF16)],
                  scratch=[pltpu.VMEM((D, D), F32)], carry=carry)(dattn, w_o, mixed, ah, aa, p, p)


def _local_step(x, ctx, tgt, mod, modc, nw1, nw2, lg, hw, qnw, knw, sinks,
                w_in, wts, dist=None):
    s_len = x.shape[0]
    tt = s_len + L
    ss1 = jnp.stack([modc, mod[0:2]])
    ss2 = mod[3:5][None]
    g1, g2 = mod[2:3], mod[5:6]
    hw4 = jnp.tile(hw, (1, 4))
    qnw8 = jnp.tile(qnw, (1, 8))
    knw2 = jnp.tile(knw, (1, 2))
    cos, sin = _rope_tables(s_len)
    bd512, bd128 = _blockdiag(ATW, HDIM), _blockdiag(128, HDIM)
    dupm = _dup_matrix()
    dup, dupt = jnp.asarray(dupm, BF16), jnp.asarray(dupm.T, F32)
    tmt = tt

    def four(b):
        return b.reshape(4, 2 * b.shape[1], b.shape[2])

    def halves(g):
        return g.reshape(4, 2, g.shape[1] // 2, g.shape[2])

    h = _mod1(ctx, x, nw1, ss1)
    if dist is None:
        bh4, ba4, w_o, g4, u4, dn4 = wts
        p = _mm_in(h, w_in, tmt)
        o0, st0 = _hgrn_fwd(p, lg, rev=False)
        o1, st1, y_hg = _hgrn_fwd(p, lg, rev=True, readout=(o0, hw4))
    else:
        core, chip = dist
        half = wts[3].shape[1] // 2
        p, first = _mm_in(h, w_in, tmt, carry=_carry_join(_carry_gather(list(wts[0:3])),
                                                          _carry_gather([wts[3]], rows=[(0, half)])))
        (o0, st0), (g8,) = _hgrn_fwd(p, lg, rev=False, carry=_carry_gather([first[3]], rows=[(half, half)]))
        (o1, st1, y_hg), (dn8a,) = _hgrn_fwd(p, lg, rev=True, readout=(o0, hw4),
                                             carry=_carry_gather([wts[5]], rows=[(0, half)]))
        bh4, ba4, w_o, g4 = four(first[0]), four(first[1]), four(first[2]).reshape(D, D), four(g8)
    qr, k4, v4 = _qk_prep(p, cos, sin, qnw8, knw2, bd512, bd128, dup)
    if dist is None:
        y_at, lse = _attn_fwd(qr, k4, v4, sinks)
    else:
        (y_at, lse), (u8, dn8) = _attn_fwd(qr, k4, v4, sinks,
                                           carry=_carry_gather([wts[4], dn8a], rows=[None, (half, half)]))
        u4, dn4 = four(u8), four(dn8)
    ah, aa, mixed = _branch_merge(y_hg, y_at, bh4, ba4, p)
    ao, x1, h2 = _out_proj_mod2(mixed, w_o, x, g1, nw2, ss2)
    a4, b4, z4 = _ffn_up(h2, g4, u4)
    sq, dx2, dyb, dg2 = _ffn_down_loss(z4, dn4, x1, g2, tgt)

    da4, db4 = _ffn_dz(dyb, dn4, a4, b4)
    g_dn = _ffn_gdn(z4, dyb)
    if dist is None:
        dh2 = _ffn_dh2(da4, db4, g4, u4)
    else:
        dn_units = [halves(g_dn)]
        dh2, dn_recv = _ffn_dh2(da4, db4, g4, u4, carry=_carry_pairx(dn_units))
        dn_pairs = _rs_pair_add(dn_units, dn_recv, core)
    if dist is None:
        g_g, g_u = _ffn_ggu(h2, da4, db4)
    else:
        (g_g, g_u), c_dn = _ffn_ggu(h2, da4, db4, carry=_carry_chipx(dn_pairs))
        red_dn = _rs_chip_add(dn_pairs, c_dn, core, chip)
    dx1, dattn, dss2, dnw2, dg1 = _mod2_bwd(x1, dh2, dx2, ao, nw2, ss2, g1)
    if dist is None:
        dp, dmh, dma, g_o = _merge_bwd(dattn, w_o, mixed, ah, aa, p)
    else:
        gu_units = [halves(g_g), halves(g_u)]
        (dp, dmh, dma, g_o), gu_recv = _merge_bwd(dattn, w_o, mixed, ah, aa, p, carry=_carry_pairx(gu_units))
        ffn_pairs = list(dn_pairs) + list(_rs_pair_add(gu_units, gu_recv, core))
    dy_hg, dy_at, g_bh, g_ba = _branch_bwd(dmh, dma, bh4, ba4, y_hg, y_at)
    if dist is None:
        dp, do, dhw4 = _readout_bwd(o0, o1, p, hw4, dy_hg, dp)
        dq, dkw, dvw, dkc, dvc, dsk = _attn_bwd(qr, k4, v4, sinks, y_at, lse, dy_at)
        dp, dqnw8, dknw2 = _attn_post(p, cos, sin, qnw8, knw2, bd512, bd128, dupt, dq, dkw, dvw, dkc, dvc, dp)
    else:
        mix_units = [halves(g_bh), halves(g_ba), halves(g_o.reshape(4, D // 4, D))]
        (dp, do, dhw4), mix_recv = _readout_bwd(o0, o1, p, hw4, dy_hg, dp, carry=_carry_pairx(mix_units))
        mix_pairs = _rs_pair_add(mix_units, mix_recv, core)
        (dq, dkw, dvw, dkc, dvc, dsk), bwd = _attn_bwd(
            qr, k4, v4, sinks, y_at, lse, dy_at,
            carry=_carry_join(_carry_chipx(ffn_pairs[1:2]), _carry_sibx(red_dn)))
        red_g = _rs_chip_add(ffn_pairs[1:2], bwd[0:1], core, chip)
        (dp, dqnw8, dknw2), post = _attn_post(
            p, cos, sin, qnw8, knw2, bd512, bd128, dupt, dq, dkw, dvw, dkc, dvc, dp, carry=_carry_sibx(red_g))
    if dist is None:
        dp, dv0, dq0, dlg0 = _hgrn_bwd(p, lg, do, st0, dp, None, rev=False)
        dp, dlg1 = _hgrn_bwd(p, lg, do, st1, dp, (dv0, dq0), rev=True)
    else:
        (dp, dv0, dq0, dlg0), c_u = _hgrn_bwd(p, lg, do, st0, dp, None, rev=False,
                                              carry=_carry_chipx(ffn_pairs[2:3]))
        red_u = _rs_chip_add(ffn_pairs[2:3], c_u, core, chip)
        (dp, dlg1), last = _hgrn_bwd(p, lg, do, st1, dp, (dv0, dq0), rev=True,
                                     carry=_carry_join(_carry_chipx(mix_pairs), _carry_sibx(red_u)))
        mix_reds = _rs_chip_add(mix_pairs, last[0:3], core, chip)
        ffn_done = bwd[1:2] + post[0:1] + last[3:4]
    g_in = _mm_gin(dp, h, tmt)
    if dist is None:
        dh = _mm_dh(dp, w_in, tmt)
        gx, dss1, dnw1 = _mod1_bwd(ctx, x, dh, dx1, nw1, ss1)
        rs = None
    else:
        in_units = [halves(g_in.reshape(4, NCOL // 4, D))]
        dh, both = _mm_dh(dp, w_in, tmt, carry=_carry_join(_carry_pairx(in_units), _carry_sibx(mix_reds)))
        in_recv, mix_done = both[0:1], both[1:4]
        in_pairs = _rs_pair_add(in_units, in_recv, core)
        first_rows = (0, in_pairs[0].shape[1] // 2)
        (gx, dss1, dnw1), in_part = _mod1_bwd(ctx, x, dh, dx1, nw1, ss1,
                                              carry=_carry_chipx(in_pairs, rows=first_rows))
        rs = dict(ffn_done=ffn_done, mix_done=mix_done, in_pairs=in_pairs, in_part=in_part)

    dmod = jnp.concatenate([dss1[1], dg1, dss2, dg2], axis=0)
    dmodc = dss1[0]
    raw = (dss1, dg1, dss2, dg2, dnw1, dnw2, dhw4, dqnw8, dknw2, dsk, dlg0, dlg1)
    small = dict(raw=raw, dmod=dmod, dmodc=dmodc, dnw1=dnw1, dnw2=dnw2,
                 dhw=dhw4.reshape(4, HGD).sum(0, keepdims=True),
                 dqnw=dqnw8.reshape(8, HDIM).sum(0, keepdims=True),
                 dknw=dknw2.reshape(2, HDIM).sum(0, keepdims=True),
                 dsinks=dsk[:, 0], dlg=jnp.concatenate([dlg0, dlg1], axis=0))
    big = dict(w_in=g_in, w_bh=g_bh, w_ba=g_ba, w_o=g_o, w_g=g_g, w_u=g_u, w_dn=g_dn)
    return sq, gx, big, small, rs


def _place():
    x, y, c = lax.axis_index("x"), lax.axis_index("y"), lax.axis_index("c")
    return x, y, c


def _gather_blocks(x_refs, out_refs, send_sems, recv_sems, local_sems):
    n = len(out_refs)
    x, y, c = _place()
    me, sibling = (x, y, c), (x, y, 1 - c)
    chips = [(1 - x, y), (x, 1 - y), (1 - x, 1 - y)]

    def slot(u, px, py, pc):
        return out_refs[u].at[4 * px + 2 * py + pc]

    def copy(u, k, block, to, src=None):
        return pltpu.make_async_remote_copy(
            src_ref=slot(u, *block) if src is None else src, dst_ref=slot(u, *block),
            send_sem=send_sems.at[u, k], recv_sem=recv_sems.at[u, k], device_id=to, device_id_type=MESH)

    mines = [pltpu.make_async_copy(x_refs[u], slot(u, *me), local_sems.at[u]) for u in range(n)]
    for cp in mines:
        cp.start()
    first = []
    for u in range(n):
        first.append(copy(u, 0, me, sibling, src=x_refs[u]))
        first += [copy(u, 1 + j, me, (*chip, c), src=x_refs[u]) for j, chip in enumerate(chips)]
    for cp in first:
        cp.start()
    passed = []
    for j, chip in enumerate(chips):
        for u in range(n):
            copy(u, 1 + j, (*chip, c), me).wait_recv()
            fwd = copy(u, 4 + j, (*chip, c), sibling)
            fwd.start()
            passed.append(fwd)
    for u in range(n):
        copy(u, 0, sibling, me).wait_recv()
    for j, chip in enumerate(chips):
        for u in range(n):
            copy(u, 4 + j, (*chip, 1 - c), me).wait_recv()
    for cp in first + passed:
        cp.wait_send()
    for cp in mines:
        cp.wait()


def _gather_sems(n):
    return [pltpu.SemaphoreType.DMA((n, 7)), pltpu.SemaphoreType.DMA((n, 7)), pltpu.SemaphoreType.DMA((n,))]


def _allgather(blks, *, name):
    n = len(blks)
    vm = pl.BlockSpec(memory_space=pltpu.VMEM)

    def body(*refs):
        _peer_barrier("both")
        _gather_blocks(refs[:n], refs[n:2 * n], *refs[2 * n:])

    return pl.pallas_call(
        body, name=name, out_shape=[jax.ShapeDtypeStruct((8,) + b.shape, b.dtype) for b in blks],
        in_specs=[vm] * n, out_specs=[vm] * n, scratch_shapes=_gather_sems(n),
        compiler_params=pltpu.CompilerParams(collective_id=BARRIER_IDS["both"]))(*blks)


def _cast_place(ws, c, dev):
    n = len(ws)

    def body(s_ref, *refs):
        for u in range(n):
            refs[n + u][0] = refs[u][...].astype(BF16)

    in_specs, out_specs, out_shape = [], [], []
    for w in ws:
        q, cols = w.shape[0] // 4, w.shape[1]
        in_specs.append(pl.BlockSpec((q, cols), lambda i, s: (2 * s[0] + i, 0)))
        out_specs.append(pl.BlockSpec((1, q, cols), lambda i, s: (s[1], i, 0)))
        out_shape.append(jax.ShapeDtypeStruct((8, 2 * q, cols), BF16))
    return pl.pallas_call(
        body, name="cast_place",
        grid_spec=pltpu.PrefetchScalarGridSpec(num_scalar_prefetch=1, grid=(2,), in_specs=in_specs,
                                               out_specs=out_specs),
        out_shape=_out_hbm(out_shape),
        compiler_params=pltpu.CompilerParams(vmem_limit_bytes=48 << 20))(jnp.stack([c, dev]), *_in_hbm(ws))


def _gather_phases(out_refs, send_sems, recv_sems, rows=None):
    n = len(out_refs)
    x, y, c = _place()
    me, sibling = (x, y, c), (x, y, 1 - c)
    chips = [(1 - x, y), (x, 1 - y), (1 - x, 1 - y)]

    def copy(u, k, block, to):
        px, py, pc = block
        ref = out_refs[u].at[4 * px + 2 * py + pc]
        if rows is not None and rows[u] is not None:
            ref = ref.at[pl.ds(rows[u][0], rows[u][1])]
        return pltpu.make_async_remote_copy(src_ref=ref, dst_ref=ref, send_sem=send_sems.at[u, k],
                                            recv_sem=recv_sems.at[u, k], device_id=to, device_id_type=MESH)

    def start():
        for u in range(n):
            copy(u, 0, me, sibling).start()
            for j, chip in enumerate(chips):
                copy(u, 1 + j, me, (*chip, c)).start()

    def mid():
        for j, chip in enumerate(chips):
            for u in range(n):
                copy(u, 1 + j, (*chip, c), me).wait_recv()
                copy(u, 4 + j, (*chip, c), sibling).start()

    def end():
        for u in range(n):
            copy(u, 0, sibling, me).wait_recv()
        for j, chip in enumerate(chips):
            for u in range(n):
                copy(u, 4 + j, (*chip, 1 - c), me).wait_recv()
        for u in range(n):
            copy(u, 0, me, sibling).wait_send()
            for j, chip in enumerate(chips):
                copy(u, 1 + j, me, (*chip, c)).wait_send()
                copy(u, 4 + j, (*chip, c), sibling).wait_send()

    return start, mid, end


def _carry_gather(bufs, rows=None):
    n = len(bufs)
    return _Carry(bufs, [jax.ShapeDtypeStruct(b.shape, b.dtype) for b in bufs], {u: u for u in range(n)},
                  [pltpu.SemaphoreType.DMA((n, 7)), pltpu.SemaphoreType.DMA((n, 7))],
                  lambda ins, outs, sems: _gather_phases(outs, *sems, rows=rows), "both")


def _ag_small(raw, sq):
    def body(dss1, dg1, dss2, dg2, dnw1, dnw2, dhw4, dqnw8, dknw2, dsk, dlg0, dlg1, sq_ref,
             out_ref, tot_ref, blk, send_sems, recv_sems, local_sems):
        _peer_barrier("both")
        blk[...] = jnp.zeros_like(blk)
        blk[0:2, :] = dss1[1]
        blk[2:3, :] = dg1[...]
        blk[3:5, :] = dss2[...]
        blk[5:6, :] = dg2[...]
        blk[6:8, :] = dss1[0]
        blk[8:9, :] = dnw1[...]
        blk[9:10, :] = dnw2[...]
        blk[10:11, 0:HGW] = dhw4[...]
        blk[10:11, HGW:D] = dqnw8[...]
        blk[11:12, 0:128] = dknw2[...]
        blk[12:14, 0:HGW] = dlg0[0]
        blk[14:16, 0:HGW] = dlg1[0]
        blk[16:24, 0:128] = dsk[...]
        blk[24:25, :] = sq_ref[...]
        _gather_blocks([blk], [out_ref], send_sems, recv_sems, local_sems)
        acc = out_ref[0]
        for i in range(1, 8):
            acc = acc + out_ref[i]
        tot_ref[...] = acc

    vm = pl.BlockSpec(memory_space=pltpu.VMEM)
    return pl.pallas_call(
        body, name="ag_small",
        out_shape=[jax.ShapeDtypeStruct((8, 32, D), F32), jax.ShapeDtypeStruct((32, D), F32)],
        in_specs=[vm] * 13, out_specs=[vm, vm],
        scratch_shapes=[pltpu.VMEM((32, D), F32)] + _gather_sems(1),
        compiler_params=pltpu.CompilerParams(collective_id=BARRIER_IDS["both"]))(*raw, sq)


def _pairx_shapes(units):
    return [jax.ShapeDtypeStruct((4,) + g.shape[2:], g.dtype) for g in units]


def _pairx_phases(g_refs, r_refs, send_sems, recv_sems):
    n = len(g_refs)
    x, y, c = _place()
    cps = [pltpu.make_async_remote_copy(
        src_ref=g_refs[u].at[j, 1 - c], dst_ref=r_refs[u].at[j], send_sem=send_sems.at[u, j],
        recv_sem=recv_sems.at[u, j], device_id=(x, y, 1 - c), device_id_type=MESH)
        for u in range(n) for j in range(4)]

    def start():
        for cp in cps:
            cp.start()

    def end():
        for cp in cps:
            cp.wait()

    return start, None, end


def _carry_pairx(units):
    n = len(units)
    return _Carry(units, _pairx_shapes(units), {},
                  [pltpu.SemaphoreType.DMA((n, 4)), pltpu.SemaphoreType.DMA((n, 4))],
                  lambda ins, outs, sems: _pairx_phases(ins, outs, *sems), "sib")


def _rs_pair_add(units, recvs, c):
    n = len(units)

    def body(c_ref, *refs):
        for u in range(n):
            refs[2 * n + u][...] = (refs[u][0].astype(F32) + refs[n + u][...].astype(F32)).astype(BF16)

    in_specs, out_specs, out_shape = [], [], []
    for g in units:
        h, w = g.shape[2] // 2, g.shape[3]
        in_specs.append(pl.BlockSpec((1, 1, h, w), lambda j, i, cr: (j, cr[0], i, 0)))
    for g in units:
        h, w = g.shape[2] // 2, g.shape[3]
        in_specs.append(pl.BlockSpec((1, h, w), lambda j, i, cr: (j, i, 0)))
        out_specs.append(pl.BlockSpec((1, h, w), lambda j, i, cr: (j, i, 0)))
        out_shape.append(jax.ShapeDtypeStruct((4, 2 * h, w), BF16))
    return pl.pallas_call(
        body, name="rs_pair_add",
        grid_spec=pltpu.PrefetchScalarGridSpec(num_scalar_prefetch=1, grid=(4, 2), in_specs=in_specs,
                                               out_specs=out_specs),
        out_shape=_out_hbm(out_shape),
        compiler_params=pltpu.CompilerParams(vmem_limit_bytes=48 << 20))(
            c.reshape(1), *_in_hbm(list(units) + list(recvs)))


def _chipx_phases(p_refs, r_refs, send_sems, recv_sems, rows=None):
    n = len(p_refs)
    x, y, c = _place()
    k = 2 * x + y

    def part(ref):
        return ref if rows is None else ref.at[pl.ds(rows[0], rows[1])]

    sends = []
    for d in range(1, 4):
        j = (k + d) % 4
        for u in range(n):
            sends.append(pltpu.make_async_remote_copy(
                src_ref=part(p_refs[u].at[j]), dst_ref=part(r_refs[u].at[k]), send_sem=send_sems.at[u, d - 1],
                recv_sem=recv_sems.at[u, d - 1], device_id=(j // 2, j % 2, c), device_id_type=MESH))

    def start():
        for cp in sends:
            cp.start()

    def end():
        for d in range(1, 4):
            src = (k + 4 - d) % 4
            for u in range(n):
                pltpu.make_async_remote_copy(
                    src_ref=part(p_refs[u].at[src]), dst_ref=part(r_refs[u].at[src]),
                    send_sem=send_sems.at[u, d - 1], recv_sem=recv_sems.at[u, d - 1], device_id=(x, y, c),
                    device_id_type=MESH).wait_recv()
        for cp in sends:
            cp.wait_send()

    return start, None, end


def _carry_chipx(pairs, rows=None, into=None):
    n = len(pairs)
    sems = [pltpu.SemaphoreType.DMA((n, 3)), pltpu.SemaphoreType.DMA((n, 3))]
    shapes = [jax.ShapeDtypeStruct(p.shape, p.dtype) for p in pairs]
    if into is None:
        return _Carry(pairs, shapes, {}, sems, lambda ins, outs, s: _chipx_phases(ins, outs, *s, rows=rows),
                      "chips")
    return _Carry(list(pairs) + list(into), shapes, {n + u: u for u in range(n)}, sems,
                  lambda ins, outs, s: _chipx_phases(ins[:n], outs, *s, rows=rows), "chips")


def _rs_chip_add(pairs, contribs, c, chip):
    n = len(pairs)

    def body(s_ref, *refs):
        for u in range(n):
            a, b, c_, d = refs[4 * u:4 * u + 4]
            refs[4 * n + u][0] = ((a[0].astype(F32) + b[0].astype(F32)) + c_[0].astype(F32)) + d[0].astype(F32)

    in_specs, out_specs, out_shape, args = [], [], [], []
    for p, r in zip(pairs, contribs):
        h, w = p.shape[1] // 2, p.shape[2]
        in_specs += [pl.BlockSpec((1, h, w), functools.partial(lambda d, i, s: ((s[1] + d) % 4, i, 0), d))
                     for d in range(4)]
        args += [p, r, r, r]
        out_specs.append(pl.BlockSpec((1, h, w), lambda i, s: (s[0], i, 0)))
        out_shape.append(jax.ShapeDtypeStruct((2, 2 * h, w), F32))
    return pl.pallas_call(
        body, name="rs_chip_add",
        grid_spec=pltpu.PrefetchScalarGridSpec(num_scalar_prefetch=1, grid=(2,), in_specs=in_specs,
                                               out_specs=out_specs),
        out_shape=_out_hbm(out_shape),
        compiler_params=pltpu.CompilerParams(vmem_limit_bytes=48 << 20))(jnp.stack([c, chip]), *_in_hbm(args))


def _rs_sibling_gather(reds):
    n = len(reds)

    def body(*refs):
        _peer_barrier("sib")
        start, _, end = _sibx_phases(refs[n:2 * n], *refs[2 * n:])
        start()
        end()

    return pl.pallas_call(
        body, name="rs_sibling_gather", out_shape=[jax.ShapeDtypeStruct(r.shape, r.dtype) for r in reds],
        in_specs=[ANY] * n, out_specs=[ANY] * n, input_output_aliases={u: u for u in range(n)},
        scratch_shapes=[pltpu.SemaphoreType.DMA((n,))] * 2,
        compiler_params=pltpu.CompilerParams(collective_id=BARRIER_IDS["sib"]))(*reds)


def _sibx_phases(o_refs, send_sems, recv_sems):
    n = len(o_refs)
    x, y, c = _place()
    cps = [pltpu.make_async_remote_copy(
        src_ref=o_refs[u].at[c], dst_ref=o_refs[u].at[c], send_sem=send_sems.at[u], recv_sem=recv_sems.at[u],
        device_id=(x, y, 1 - c), device_id_type=MESH) for u in range(n)]

    def start():
        for cp in cps:
            cp.start()

    def end():
        for u in range(n):
            cps[u].wait_send()
            pltpu.make_async_remote_copy(
                src_ref=o_refs[u].at[1 - c], dst_ref=o_refs[u].at[1 - c], send_sem=send_sems.at[u],
                recv_sem=recv_sems.at[u], device_id=(x, y, 1 - c), device_id_type=MESH).wait_recv()

    return start, None, end


def _carry_sibx(reds):
    n = len(reds)
    return _Carry(reds, [jax.ShapeDtypeStruct(r.shape, r.dtype) for r in reds], {u: u for u in range(n)},
                  [pltpu.SemaphoreType.DMA((n,))] * 2, lambda ins, outs, sems: _sibx_phases(outs, *sems), "sib")


def _prologue(blk, c_ctx, w, b, in8):
    n = w.shape[1]

    def body(blk_ref, cctx_ref, w_ref, b_ref, _in_in, g0_ref, c16_ref, g1_ref, in_ref, mod_s,
             s1, r1, l1, s2, r2, l2, s3, r3):
        _peer_barrier("both")
        start, mid, end = _gather_phases([in_ref], s3, r3)
        _gather_blocks([blk_ref], [g0_ref], s1, r1, l1)
        start()
        c16 = jnp.concatenate([g0_ref[i, 0:1, :] for i in range(8)] + [cctx_ref[...], jnp.zeros((7, D), F32)],
                              axis=0)
        c16_ref[...] = c16
        mod_s[...] = _dot(c16 * _sig(c16), w_ref[...], prec=HI) + b_ref[...]
        _gather_blocks([mod_s], [g1_ref], s2, r2, l2)
        mid()
        end()

    vm = pl.BlockSpec(memory_space=pltpu.VMEM)
    return pl.pallas_call(
        body, name="prologue",
        out_shape=[jax.ShapeDtypeStruct((8, 8, D), F32), jax.ShapeDtypeStruct((16, D), F32),
                   jax.ShapeDtypeStruct((8, 16, n), F32), jax.ShapeDtypeStruct(in8.shape, in8.dtype)],
        in_specs=[vm, vm, vm, vm, ANY], out_specs=[vm, vm, vm, ANY], input_output_aliases={4: 3},
        scratch_shapes=[pltpu.VMEM((16, n), F32)] + _gather_sems(1) + _gather_sems(1)
        + [pltpu.SemaphoreType.DMA((1, 7)), pltpu.SemaphoreType.DMA((1, 7))],
        compiler_params=pltpu.CompilerParams(vmem_limit_bytes=48 << 20,
                                             collective_id=BARRIER_IDS["both"]))(blk, c_ctx, w, b, in8)


def _ada_bwd(c16, dmod16, w, carry=None):
    n = w.shape[1]
    tn = 512

    def body(c_ref, d_ref, w_ref, gw_ref, gc_ref):
        j = pl.program_id(0)

        @pl.when(j == 0)
        def _():
            gc_ref[...] = jnp.zeros_like(gc_ref)

        cc = c_ref[...]
        dm = d_ref[...]
        gw_ref[...] = _dot(cc * _sig(cc), dm, TN, prec=HI)
        gc_ref[...] += _dot(dm, w_ref[...], NT, prec=HI)

    return _pcall(body, name="ada_bwd", grid=(n // tn,),
                  in_specs=[_full((16, D)), pl.BlockSpec((16, tn), lambda j: (0, j)),
                            pl.BlockSpec((D, tn), lambda j: (0, j))],
                  out_specs=[pl.BlockSpec((D, tn), lambda j: (0, j)), _full((16, D))],
                  out_shape=[jax.ShapeDtypeStruct((D, n), F32),
                             jax.ShapeDtypeStruct((16, D), F32)], carry=carry)(c16, dmod16, w)


def _adam_math(w, g, m, v):
    c1 = 1.0 - ADAM_B1 ** ADAM_STEP
    c2 = 1.0 - ADAM_B2 ** ADAM_STEP
    nm = ADAM_B1 * m + (1.0 - ADAM_B1) * g
    nv = ADAM_B2 * v + (1.0 - ADAM_B2) * (g * g)
    return -ADAM_LR * ((nm / c1) / (jnp.sqrt(nv / c2) + ADAM_EPS) + ADAM_WD * w), nm, nv


def _adamw_small(ws, gs, ms, vs):
    n = len(ws)

    def body(*refs):
        for u in range(n):
            d_, nm, nv = _adam_math(refs[u][...], refs[n + u][...], refs[2 * n + u][...], refs[3 * n + u][...])
            refs[4 * n + u][...] = d_
            refs[5 * n + u][...] = nm
            refs[6 * n + u][...] = nv

    specs = [_full(w.shape) for w in ws]
    shapes = [jax.ShapeDtypeStruct(w.shape, F32) for w in ws]
    out = _pcall(body, name="adamw_small", grid=(1,), in_specs=specs * 4, out_specs=specs * 3,
                 out_shape=shapes * 3)(*ws, *gs, *ms, *vs)
    return out[:n], out[n:2 * n], out[2 * n:]


def _cctx_grad(parts, c_ctx):
    def body(p_ref, c_ref, o_ref):
        acc = p_ref[0:1, :]
        for k in range(1, 4):
            acc = acc + p_ref[k:k + 1, :]
        cc = c_ref[...]
        s = _sig(cc)
        o_ref[...] = acc * (s * (1.0 + cc * (1.0 - s)))

    return _pcall(body, name="cctx_grad", grid=(1,), in_specs=[_full(parts.shape), _full((1, D))],
                  out_specs=_full((1, D)), out_shape=jax.ShapeDtypeStruct((1, D), F32))(parts, c_ctx)


ADAM_STEPS = 8


def _adamw_multi(ws, gs, ms, vs, *, name, carry=None):
    n = len(ws)

    def body(*refs):
        for u in range(n):
            refs[4 * n + u][...], refs[5 * n + u][...], refs[6 * n + u][...] = _adam_math(
                refs[u][...], refs[n + u][...], refs[2 * n + u][...], refs[3 * n + u][...])

    specs = [pl.BlockSpec((w.shape[0] // ADAM_STEPS, w.shape[1]), lambda i: (i, 0)) for w in ws]
    shapes = [jax.ShapeDtypeStruct(w.shape, F32) for w in ws]
    res = _pcall(body, name=name, grid=(ADAM_STEPS,), in_specs=specs * 4, out_specs=specs * 3,
                 out_shape=shapes * 3, carry=carry)(*ws, *gs, *ms, *vs)
    out, extra = res if carry is not None else (res, None)
    return (out[:n], out[n:2 * n], out[2 * n:]), extra


def kernel(x, c, ctx, c_ctx, w_ada, b_ada, norm_mix_w, norm_ffn_w, w_in, hgrn_lb_logits, hgrn_norm_w, q_norm_w, k_norm_w, attn_sinks, w_branch_hgrn, w_branch_attn, w_out, w_ffn_gate, w_ffn_up, w_ffn_down, loss_target, m_c_ctx, m_w_ada, m_b_ada, m_norm_mix_w, m_norm_ffn_w, m_w_in, m_hgrn_lb_logits, m_hgrn_norm_w, m_q_norm_w, m_k_norm_w, m_attn_sinks, m_w_branch_hgrn, m_w_branch_attn, m_w_out, m_w_ffn_gate, m_w_ffn_up, m_w_ffn_down, v_c_ctx, v_w_ada, v_b_ada, v_norm_mix_w, v_norm_ffn_w, v_w_in, v_hgrn_lb_logits, v_hgrn_norm_w, v_q_norm_w, v_k_norm_w, v_attn_sinks, v_w_branch_hgrn, v_w_branch_attn, v_w_out, v_w_ffn_gate, v_w_ffn_up, v_w_ffn_down):
    xi, yi, ci = _place()
    chip = 2 * xi + yi
    dev = 2 * chip + ci
    s_len = x.shape[1]

    shards = [w_in[0].T, w_branch_hgrn[0], w_branch_attn[0], w_out[0], w_ffn_gate[0].T, w_ffn_up[0].T,
              w_ffn_down[0]]
    bufs = _cast_place(shards, ci, dev)

    lbrow = jnp.pad(hgrn_lb_logits.reshape(1, 512), ((0, 0), (0, D - 512)))
    blk = jnp.concatenate([c, lbrow, jnp.zeros((6, D), F32)], axis=0)
    nada = w_ada.shape[2]
    b_sh = lax.dynamic_slice(b_ada, (0, chip * nada), (1, nada))
    g0, c16, g1, in8 = _prologue(blk, c_ctx[None], w_ada[0], b_sh, bufs[0])
    lg = g0[0::2, 1, :512].reshape(4, 2, 2, 128).transpose(1, 2, 0, 3).reshape(2, 2, HGW)
    modall = g1[0::2].transpose(1, 0, 2).reshape(16, 4 * nada)
    mod = lax.dynamic_slice(modall, (dev, 0), (1, 6 * D)).reshape(6, D)
    modc = modall[8].reshape(6, D)[:2]

    sq, gx, _, small, rs = _local_step(
        x[0], ctx[0], loss_target[0], mod, modc, norm_mix_w, norm_ffn_w, lg, hgrn_norm_w, q_norm_w,
        k_norm_w, attn_sinks[0], in8.reshape(NCOL, D), bufs[1:], dist=(ci, chip))

    def whole(r):
        return r.reshape(2 * r.shape[1], r.shape[2])

    g_dn, g_g, g_u = [whole(r) for r in rs["ffn_done"]]
    g_bh, g_ba, g_o = [whole(r) for r in rs["mix_done"]]
    in_pairs = rs["in_pairs"]
    rest_rows = (in_pairs[0].shape[1] // 2, in_pairs[0].shape[1] // 2)

    g2, tot = _ag_small(small["raw"], sq)
    loss = 0.5 * jnp.sum(tot[24]) / D
    dmodc_tot = jnp.pad(tot[6:8].reshape(1, 2 * D), ((0, 0), (0, 4 * D)))
    g_b_ada = tot[0:6].reshape(1, 6 * D) + dmodc_tot
    dmod16 = jnp.concatenate([g2[:, 0:6].reshape(8, 6 * D), dmodc_tot, jnp.zeros((7, 6 * D), F32)], axis=0)
    (g_w_ada, gc_part), in_contribs = _ada_bwd(
        c16, lax.dynamic_slice(dmod16, (0, chip * nada), (16, nada)), w_ada[0],
        carry=_carry_chipx(in_pairs, rows=rest_rows, into=rs["in_part"]))
    g3, = _allgather([gc_part[8:16]], name="ag_cctx")
    g_c_ctx = _cctx_grad(g3[0::2, 0], c_ctx[None])[0]
    g_nw1 = tot[8:9]
    g_nw2 = tot[9:10]
    g_hw = tot[10, :HGW].reshape(4, HGD).sum(0, keepdims=True)
    g_qnw = tot[10, HGW:].reshape(8, HDIM).sum(0, keepdims=True)
    g_knw = tot[11, :128].reshape(2, HDIM).sum(0, keepdims=True)
    g_sinks = tot[16:24, 0][None]
    g_lg = lax.dynamic_slice(tot[12:16, :HGW].reshape(2, 2, HGW), (0, 0, chip * 128), (2, 2, 128))

    names = ["c_ctx", "w_ada", "b_ada", "norm_mix_w", "norm_ffn_w", "w_in", "hgrn_lb_logits", "hgrn_norm_w",
             "q_norm_w", "k_norm_w", "attn_sinks", "w_branch_hgrn", "w_branch_attn", "w_out", "w_ffn_gate",
             "w_ffn_up", "w_ffn_down"]
    ws = dict(zip(names, [c_ctx, w_ada, b_ada, norm_mix_w, norm_ffn_w, w_in, hgrn_lb_logits, hgrn_norm_w,
                          q_norm_w, k_norm_w, attn_sinks, w_branch_hgrn, w_branch_attn, w_out, w_ffn_gate,
                          w_ffn_up, w_ffn_down]))
    ms = dict(zip(names, [m_c_ctx, m_w_ada, m_b_ada, m_norm_mix_w, m_norm_ffn_w, m_w_in, m_hgrn_lb_logits,
                          m_hgrn_norm_w, m_q_norm_w, m_k_norm_w, m_attn_sinks, m_w_branch_hgrn,
                          m_w_branch_attn, m_w_out, m_w_ffn_gate, m_w_ffn_up, m_w_ffn_down]))
    vs = dict(zip(names, [v_c_ctx, v_w_ada, v_b_ada, v_norm_mix_w, v_norm_ffn_w, v_w_in, v_hgrn_lb_logits,
                          v_hgrn_norm_w, v_q_norm_w, v_k_norm_w, v_attn_sinks, v_w_branch_hgrn,
                          v_w_branch_attn, v_w_out, v_w_ffn_gate, v_w_ffn_up, v_w_ffn_down]))
    transposed = ("w_in", "w_ffn_gate", "w_ffn_up")

    def view(a, n):
        return a[0].T if n in transposed else a[0]

    def unview(a, n):
        return a.T[None] if n in transposed else a[None]

    delta, new_m, new_v, grads = {}, {}, {}, {}

    def big_adamw(group, gs, name, carry=None):
        (d_, m_, v_), extra = _adamw_multi([view(ws[n], n) for n in group], gs, [view(ms[n], n) for n in group],
                                           [view(vs[n], n) for n in group], name=name, carry=carry)
        for i, n in enumerate(group):
            grads[n], delta[n], new_m[n], new_v[n] = (unview(gs[i], n), unview(d_[i], n), unview(m_[i], n),
                                                      unview(v_[i], n))
        return extra

    big_adamw(["w_ffn_down", "w_ffn_gate", "w_ffn_up", "w_out", "w_branch_hgrn", "w_branch_attn"],
              [g_dn, g_g, g_u, g_o, g_bh, g_ba], "adamw_first")
    in_reds = _rs_chip_add(in_pairs, in_contribs, ci, chip)
    g_in, = [whole(r) for r in _rs_sibling_gather(in_reds)]
    big_adamw(["w_in", "w_ada"], [g_in, g_w_ada], "adamw_second")
    grads.update(c_ctx=g_c_ctx, b_ada=g_b_ada, norm_mix_w=g_nw1, norm_ffn_w=g_nw2, hgrn_lb_logits=g_lg,
                 hgrn_norm_w=g_hw, q_norm_w=g_qnw, k_norm_w=g_knw, attn_sinks=g_sinks)
    small_names = [n for n in names if n not in delta]

    def two_d(a):
        return a.reshape(1, -1) if a.ndim == 1 else a

    sd, sm_, sv = _adamw_small(*[[two_d(d[n]) for n in small_names] for d in (ws, grads, ms, vs)])
    for i, n in enumerate(small_names):
        for dst, src in ((delta, sd), (new_m, sm_), (new_v, sv)):
            dst[n] = src[i].reshape(ws[n].shape)
    return (loss, gx[None], *[grads[n] for n in names], *[delta[n] for n in names],
            *[new_m[n] for n in names], *[new_v[n] for n in names])
```

```python
import functools

import numpy as np
import jax
import jax.numpy as jnp
from jax import lax
from jax.experimental import pallas as pl
from jax.experimental.pallas import tpu as pltpu

F32 = jnp.float32
BF16 = jnp.bfloat16
HI = lax.Precision.HIGHEST
MESH = pl.DeviceIdType.MESH

D = 1024
L = 256
TM = 256
HGW = 512
HGD = 128
CH = 32
ATW = 512
HDIM = 64
BLK = 128
GRID_W = 64
DFF = 2816
NCOL = 5376
EPS = 1e-6
ROPE_THETA = 10000.0

C_FB, C_INP, C_QHG, C_FF = 0, 1, 2, 3
C_GATES = 1
C_GHG, C_QRAW = 8, 9
C_KV = 20
C_QKV = 6

ADAM_LR, ADAM_B1, ADAM_B2, ADAM_EPS, ADAM_WD, ADAM_STEP = 0.001, 0.9, 0.999, 1e-08, 0.01, 10

NN = (((1,), (0,)), ((), ()))
NT = (((1,), (1,)), ((), ()))
TN = (((0,), (0,)), ((), ()))


def _dot(a, b, dims=NN, prec=None):
    return lax.dot_general(a, b, dims, precision=prec, preferred_element_type=F32)


def _bdot(a, b, dims=NN):
    return _dot(a.astype(BF16), b.astype(BF16), dims)


def _sig(x):
    return 1.0 / (1.0 + jnp.exp(-x))


class _Carry:
    def __init__(self, ins, outs, aliases, scratch, phases, peers):
        self.ins, self.outs, self.aliases, self.scratch, self.phases = ins, outs, aliases, scratch, phases
        self.peers = peers


BARRIER_IDS = {"sib": 1, "chips": 2, "both": 3}


def _peer_barrier(kind):
    x, y, c = _place()
    peers = []
    if kind in ("sib", "both"):
        peers.append((x, y, 1 - c))
    if kind in ("chips", "both"):
        peers += [(1 - x, y, c), (x, 1 - y, c), (1 - x, 1 - y, c)]
    bar = pltpu.get_barrier_semaphore()
    for peer in peers:
        pl.semaphore_signal(bar, inc=1, device_id=peer, device_id_type=MESH)
    pl.semaphore_wait(bar, len(peers))


def _in_hbm(args):
    return [pltpu.with_memory_space_constraint(a, pltpu.HBM) for a in args]


def _out_hbm(shapes):
    if isinstance(shapes, (list, tuple)):
        return [pltpu.HBM(s.shape, s.dtype) for s in shapes]
    return pltpu.HBM(shapes.shape, shapes.dtype)


def _carry_join(a, b):
    na_in, na_out, na_sc = len(a.ins), len(a.outs), len(a.scratch)
    aliases = dict(a.aliases)
    aliases.update({na_in + i: na_out + o for i, o in b.aliases.items()})

    def phases(ins, outs, sems):
        pa = a.phases(ins[:na_in], outs[:na_out], sems[:na_sc])
        pb = b.phases(ins[na_in:], outs[na_out:], sems[na_sc:])

        def both(fa, fb):
            if fa is None and fb is None:
                return None

            def run():
                for fn in (fa, fb):
                    if fn is not None:
                        fn()
            return run

        return tuple(both(fa, fb) for fa, fb in zip(pa, pb))

    return _Carry(list(a.ins) + list(b.ins), list(a.outs) + list(b.outs), aliases,
                  list(a.scratch) + list(b.scratch), phases, a.peers if a.peers == b.peers else "both")


def _pcall(body, *, name, grid, in_specs, out_specs, out_shape, scratch=(), aliases=None, vmem_mb=48,
           carry=None):
    params = pltpu.CompilerParams(dimension_semantics=("arbitrary",) * len(grid),
                                  vmem_limit_bytes=vmem_mb << 20)
    if carry is None:
        plain = pl.pallas_call(
            body, name=name, grid=grid, in_specs=in_specs, out_specs=out_specs, out_shape=_out_hbm(out_shape),
            scratch_shapes=list(scratch), input_output_aliases=aliases or {}, compiler_params=params)
        return lambda *args: plain(*_in_hbm(args))
    single = not isinstance(out_shape, (list, tuple))
    out_specs_l = [out_specs] if single else list(out_specs)
    out_shape_l = [out_shape] if single else list(out_shape)
    n_in, n_out, n_sc = len(in_specs), len(out_shape_l), len(scratch)
    k_in, k_out = len(carry.ins), len(carry.outs)
    nsteps = int(np.prod(grid))
    assert nsteps >= 3

    def wrapped(*refs):
        ins, cins = refs[:n_in], refs[n_in:n_in + k_in]
        o0 = n_in + k_in
        outs, couts = refs[o0:o0 + n_out], refs[o0 + n_out:o0 + n_out + k_out]
        s0 = o0 + n_out + k_out
        sc, csc = refs[s0:s0 + n_sc], refs[s0 + n_sc:]
        step = pl.program_id(0)
        for ax in range(1, len(grid)):
            step = step * grid[ax] + pl.program_id(ax)
        start, mid, end = carry.phases(cins, couts, csc)

        @pl.when(step == 0)
        def _():
            _peer_barrier(carry.peers)
            start()

        body(*ins, *outs, *sc)
        if mid is not None:
            pl.when(step == nsteps - 2)(mid)
        pl.when(step == nsteps - 1)(end)

    all_aliases = dict(aliases or {})
    all_aliases.update({n_in + i: n_out + o for i, o in carry.aliases.items()})
    call = pl.pallas_call(
        wrapped, name=name, grid=grid, in_specs=list(in_specs) + [ANY] * k_in,
        out_specs=out_specs_l + [ANY] * k_out, out_shape=_out_hbm(out_shape_l + list(carry.outs)),
        scratch_shapes=list(scratch) + list(carry.scratch), input_output_aliases=all_aliases,
        compiler_params=pltpu.CompilerParams(dimension_semantics=("arbitrary",) * len(grid),
                                             vmem_limit_bytes=vmem_mb << 20,
                                             collective_id=BARRIER_IDS[carry.peers]))

    def run(*args):
        res = call(*_in_hbm(args), *carry.ins)
        core = res[:n_out]
        return (core[0] if single else list(core)), list(res[n_out:])

    return run


def _full(shape):
    nd = len(shape)
    return pl.BlockSpec(shape, lambda *_: (0,) * nd)


ANY = pl.BlockSpec(memory_space=pl.ANY)


NT_IN = NCOL // 256


def _src_block(j):
    return j + jnp.where(j < 4, 2, jnp.where(j < 6, 3, jnp.where(j < 8, -6, jnp.where(
        j < 16, 5, jnp.where(j < 20, -7, -14)))))


def _mm_in(h, wt, tm, carry=None):
    tt = h.shape[0]

    def body(h_ref, w_ref, o_ref):
        o_ref[...] = _bdot(h_ref[...], w_ref[...], NT)

    return _pcall(body, name="mm_in", grid=(tt // tm, NT_IN),
                  in_specs=[pl.BlockSpec((tm, D), lambda i, j: (i, 0)),
                            pl.BlockSpec((256, D), lambda i, j: (_src_block(j), 0))],
                  out_specs=pl.BlockSpec((tm, 256), lambda i, j: (i, j)),
                  out_shape=jax.ShapeDtypeStruct((tt, NCOL), F32), carry=carry)(h, wt)


def _mm_dh(dp, wt, tm, carry=None):
    tt = dp.shape[0]
    per, ng = 3, NT_IN // 3

    def body(d_ref, w0, w1, w2, o_ref, acc):
        kk = pl.program_id(1)

        @pl.when(kk == 0)
        def _():
            acc[...] = jnp.zeros_like(acc)

        acc[...] += (_bdot(d_ref[:, 0:256], w0[...]) + _bdot(d_ref[:, 256:512], w1[...])
                     + _bdot(d_ref[:, 512:768], w2[...]))

        @pl.when(kk == ng - 1)
        def _():
            o_ref[...] = acc[...]

    wspecs = [pl.BlockSpec((256, D), functools.partial(lambda t, i, kk: (_src_block(per * kk + t), 0), t))
              for t in range(per)]
    return _pcall(body, name="mm_dh", grid=(tt // tm, ng),
                  in_specs=[pl.BlockSpec((tm, per * 256), lambda i, kk: (i, kk))] + wspecs,
                  out_specs=pl.BlockSpec((tm, D), lambda i, kk: (i, 0)),
                  out_shape=jax.ShapeDtypeStruct((tt, D), F32), scratch=[pltpu.VMEM((tm, D), F32)],
                  carry=carry)(dp, wt, wt, wt)


def _mm_gin(dp, h, tk):
    tt = dp.shape[0]
    nk = tt // tk

    def body(d_ref, h_ref, o_ref, acc):
        kk = pl.program_id(1)

        @pl.when(kk == 0)
        def _():
            acc[...] = jnp.zeros_like(acc)

        acc[...] += _bdot(d_ref[...], h_ref[...], TN)

        @pl.when(kk == nk - 1)
        def _():
            o_ref[...] = acc[...].astype(BF16)

    return _pcall(body, name="mm_gin", grid=(NT_IN, nk),
                  in_specs=[pl.BlockSpec((tk, 256), lambda j, kk: (kk, j)),
                            pl.BlockSpec((tk, D), lambda j, kk: (kk, 0))],
                  out_specs=pl.BlockSpec((256, D), lambda j, kk: (_src_block(j), 0)),
                  out_shape=jax.ShapeDtypeStruct((NCOL, D), BF16), scratch=[pltpu.VMEM((256, D), F32)])(dp, h)


def _tok_specs():
    assert L == TM
    return [_full((TM, D)), pl.BlockSpec((TM, D), lambda i: (jnp.maximum(i - 1, 0), 0))]


def _mod1(ctx, x, nw, ss):
    rows = L + x.shape[0]

    def body(c_ref, x_ref, nw_ref, ss_ref, h_ref):
        t = jnp.where(pl.program_id(0) == 0, c_ref[...], x_ref[...])
        r = lax.rsqrt(jnp.mean(t * t, axis=-1, keepdims=True) + EPS)
        s = ss_ref[0]
        h_ref[...] = ((t * r * nw_ref[...]) * (1.0 + s[1:2]) + s[0:1]).astype(BF16)

    return _pcall(body, name="mod1", grid=(rows // TM,),
                  in_specs=_tok_specs() + [_full((1, D)),
                                           pl.BlockSpec((1, 2, D), lambda i: (jnp.minimum(i, 1), 0, 0))],
                  out_specs=pl.BlockSpec((TM, D), lambda i: (i, 0)),
                  out_shape=jax.ShapeDtypeStruct((rows, D), BF16))(ctx, x, nw, ss)


def _norm_bwd_rows(x, dh, nw, scale):
    r = lax.rsqrt(jnp.mean(x * x, axis=-1, keepdims=True) + EPS)
    xh = x * r
    dxh = dh * ((1.0 + scale) * nw)
    dx = r * (dxh - xh * jnp.mean(dxh * xh, axis=-1, keepdims=True))
    return dx, xh


def _out_proj_mod2(mixed, w_o, x, g1, nw2, ss2):
    s_len = x.shape[0]
    tm = 512

    def body(m_ref, w_ref, x_ref, g_ref, nw_ref, ss_ref, ao_ref, x1_ref, h_ref):
        ao = _bdot(m_ref[...], w_ref[...])
        ao_ref[...] = ao.astype(BF16)
        x1 = x_ref[...] + g_ref[...] * ao
        x1_ref[...] = x1
        r = lax.rsqrt(jnp.mean(x1 * x1, axis=-1, keepdims=True) + EPS)
        s = ss_ref[0]
        h_ref[...] = ((x1 * r * nw_ref[...]) * (1.0 + s[1:2]) + s[0:1]).astype(BF16)

    row = pl.BlockSpec((tm, D), lambda i: (i, 0))
    f = jax.ShapeDtypeStruct((s_len, D), F32)
    return _pcall(body, name="out_proj_mod2", grid=(s_len // tm,),
                  in_specs=[row, _full((D, D)), row, _full((1, D)), _full((1, D)), _full((1, 2, D))],
                  out_specs=[row, row, row],
                  out_shape=[jax.ShapeDtypeStruct((s_len, D), BF16), f,
                             jax.ShapeDtypeStruct((s_len, D), BF16)])(mixed, w_o, x, g1, nw2, ss2)


TS = 2048
TSD = 1024


def _acc_call(body, *, name, grid, in_specs, out_specs, out_shape, acc_shapes, args, carry=None):
    return _pcall(body, name=name, grid=grid, in_specs=in_specs, out_specs=out_specs, out_shape=out_shape,
                  scratch=[pltpu.VMEM(s, F32) for s in acc_shapes], carry=carry)(*args)


def _ffn_up(h2, g4, u4, carry=None):
    s_len = h2.shape[0]
    ns = g4.shape[1]

    def body(h_ref, g_ref, u_ref, a_ref, b_ref, z_ref):
        h = h_ref[...]
        a = _bdot(h, g_ref[0], NT)
        b = _bdot(h, u_ref[0], NT)
        a_ref[0] = a.astype(BF16)
        b_ref[0] = b.astype(BF16)
        z_ref[0] = (a * _sig(a) * b).astype(BF16)

    w = pl.BlockSpec((1, ns, D), lambda i, j: (j, 0, 0))
    o = pl.BlockSpec((1, TS, ns), lambda i, j: (j, i, 0))
    f = jax.ShapeDtypeStruct((4, s_len, ns), BF16)
    return _pcall(body, name="ffn_up", grid=(s_len // TS, 4),
                  in_specs=[pl.BlockSpec((TS, D), lambda i, j: (i, 0)), w, w], out_specs=[o, o, o],
                  out_shape=[f, f, jax.ShapeDtypeStruct((4, s_len, ns), BF16)], carry=carry)(h2, g4, u4)


def _ffn_down_loss(z4, dn4, x1, g2, tgt):
    _, s_len, ns = z4.shape

    def body(z_ref, w_ref, x1_ref, g_ref, t_ref, sq_ref, dx2_ref, dyb_ref, dg_ref, acc):
        i, j = pl.program_id(0), pl.program_id(1)

        @pl.when((i == 0) & (j == 0))
        def _():
            sq_ref[...] = jnp.zeros_like(sq_ref)
            dg_ref[...] = jnp.zeros_like(dg_ref)

        @pl.when(j == 0)
        def _():
            acc[...] = jnp.zeros_like(acc)

        acc[...] += _bdot(z_ref[0], w_ref[0])

        @pl.when(j == 3)
        def _():
            y_ = acc[...]
            g = g_ref[...]
            e = x1_ref[...] + g * y_ - t_ref[...]
            sq_ref[...] += jnp.sum(e * e, axis=0, keepdims=True)
            dx2 = e * (1.0 / D)
            dx2_ref[...] = dx2
            dyb_ref[...] = (g * dx2).astype(BF16)
            dg_ref[...] += jnp.sum(dx2 * y_, axis=0, keepdims=True)

    row = pl.BlockSpec((TSD, D), lambda i, j: (i, 0))
    vec = _full((1, D))
    return _acc_call(body, name="ffn_down_loss", grid=(s_len // TSD, 4),
                     in_specs=[pl.BlockSpec((1, TSD, ns), lambda i, j: (j, i, 0)),
                               pl.BlockSpec((1, ns, D), lambda i, j: (j, 0, 0)), row, vec, row],
                     out_specs=[vec, row, row, vec],
                     out_shape=[jax.ShapeDtypeStruct((1, D), F32), jax.ShapeDtypeStruct((s_len, D), F32),
                                jax.ShapeDtypeStruct((s_len, D), BF16), jax.ShapeDtypeStruct((1, D), F32)],
                     acc_shapes=[(TSD, D)], args=(z4, dn4, x1, g2, tgt))


def _ffn_dz(dyb, dn4, a4, b4):
    _, s_len, ns = a4.shape

    def body(dy_ref, w_ref, a_ref, b_ref, da_ref, db_ref):
        dz = _bdot(dy_ref[...], w_ref[0], NT)
        a = a_ref[0].astype(F32)
        s = _sig(a)
        da_ref[0] = (dz * b_ref[0].astype(F32) * (s * (1.0 + a * (1.0 - s)))).astype(BF16)
        db_ref[0] = (dz * (a * s)).astype(BF16)

    t = pl.BlockSpec((1, TS, ns), lambda i, j: (j, i, 0))
    o = jax.ShapeDtypeStruct((4, s_len, ns), BF16)
    return _pcall(body, name="ffn_dz", grid=(s_len // TS, 4),
                  in_specs=[pl.BlockSpec((TS, D), lambda i, j: (i, 0)),
                            pl.BlockSpec((1, ns, D), lambda i, j: (j, 0, 0)), t, t],
                  out_specs=[t, t], out_shape=[o, o])(dyb, dn4, a4, b4)


def _ffn_gdn(z4, dyb):
    _, s_len, ns = z4.shape
    nk = s_len // TS

    def body(z_ref, dy_ref, o_ref, acc):
        t = pl.program_id(1)

        @pl.when(t == 0)
        def _():
            acc[...] = jnp.zeros_like(acc)

        acc[...] += _bdot(z_ref[0], dy_ref[...], TN)

        @pl.when(t == nk - 1)
        def _():
            o_ref[0] = acc[...].astype(o_ref.dtype)

    return _acc_call(body, name="ffn_gdn", grid=(4, nk),
                     in_specs=[pl.BlockSpec((1, TS, ns), lambda j, t: (j, t, 0)),
                               pl.BlockSpec((TS, D), lambda j, t: (t, 0))],
                     out_specs=pl.BlockSpec((1, ns, D), lambda j, t: (j, 0, 0)),
                     out_shape=jax.ShapeDtypeStruct((4, ns, D), BF16), acc_shapes=[(ns, D)], args=(z4, dyb))


def _ffn_dh2(da4, db4, g4, u4, carry=None):
    _, s_len, ns = da4.shape

    def body(da_ref, db_ref, g_ref, u_ref, o_ref, acc):
        j = pl.program_id(1)

        @pl.when(j == 0)
        def _():
            acc[...] = jnp.zeros_like(acc)

        acc[...] += _bdot(da_ref[0], g_ref[0]) + _bdot(db_ref[0], u_ref[0])

        @pl.when(j == 3)
        def _():
            o_ref[...] = acc[...]

    t = pl.BlockSpec((1, TS, ns), lambda i, j: (j, i, 0))
    w = pl.BlockSpec((1, ns, D), lambda i, j: (j, 0, 0))
    return _acc_call(body, name="ffn_dh2", grid=(s_len // TS, 4), in_specs=[t, t, w, w],
                     out_specs=pl.BlockSpec((TS, D), lambda i, j: (i, 0)),
                     out_shape=jax.ShapeDtypeStruct((s_len, D), F32), acc_shapes=[(TS, D)],
                     args=(da4, db4, g4, u4), carry=carry)


def _ffn_ggu(h2, da4, db4, carry=None):
    _, s_len, ns = da4.shape
    nk = s_len // TS

    def body(h_ref, da_ref, db_ref, gg_ref, gu_ref, acc_g, acc_u):
        t = pl.program_id(1)

        @pl.when(t == 0)
        def _():
            acc_g[...] = jnp.zeros_like(acc_g)
            acc_u[...] = jnp.zeros_like(acc_u)

        h = h_ref[...]
        acc_g[...] += _bdot(da_ref[0], h, TN)
        acc_u[...] += _bdot(db_ref[0], h, TN)

        @pl.when(t == nk - 1)
        def _():
            gg_ref[0] = acc_g[...].astype(BF16)
            gu_ref[0] = acc_u[...].astype(BF16)

    d = pl.BlockSpec((1, TS, ns), lambda j, t: (j, t, 0))
    o = pl.BlockSpec((1, ns, D), lambda j, t: (j, 0, 0))
    f = jax.ShapeDtypeStruct((4, ns, D), BF16)
    return _acc_call(body, name="ffn_ggu", grid=(4, nk),
                     in_specs=[pl.BlockSpec((TS, D), lambda j, t: (t, 0)), d, d], out_specs=[o, o],
                     out_shape=[f, f], acc_shapes=[(ns, D), (ns, D)], args=(h2, da4, db4), carry=carry)


def _mod2_bwd(x1, dh2, dx2, ao, nw2, ss2, g1):
    s_len = x1.shape[0]

    def body(x1_ref, dh_ref, dx2_ref, ao_ref, nw_ref, ss_ref, g_ref,
             dx1_ref, da_ref, dss_ref, dnw_ref, dg_ref):
        i = pl.program_id(0)

        @pl.when(i == 0)
        def _():
            dss_ref[...] = jnp.zeros_like(dss_ref)
            dnw_ref[...] = jnp.zeros_like(dnw_ref)
            dg_ref[...] = jnp.zeros_like(dg_ref)

        dh = dh_ref[...]
        nw = nw_ref[...]
        scale = ss_ref[0][1:2]
        dxn, xh = _norm_bwd_rows(x1_ref[...], dh, nw, scale)
        dx1 = dx2_ref[...] + dxn
        dx1_ref[...] = dx1
        da_ref[...] = (g_ref[...] * dx1).astype(BF16)
        dg_ref[...] += jnp.sum(dx1 * ao_ref[...].astype(F32), axis=0, keepdims=True)
        dsh = jnp.sum(dh, axis=0, keepdims=True)
        dsc = jnp.sum(dh * xh * nw, axis=0, keepdims=True)
        dss_ref[...] += jnp.concatenate([dsh, dsc], axis=0)
        dnw_ref[...] += jnp.sum(dh * xh * (1.0 + scale), axis=0, keepdims=True)

    row = pl.BlockSpec((TM, D), lambda i: (i, 0))
    vec = _full((1, D))
    return _pcall(body, name="mod2_bwd", grid=(s_len // TM,),
                  in_specs=[row, row, row, row, vec, _full((1, 2, D)), vec],
                  out_specs=[row, row, _full((2, D)), vec, vec],
                  out_shape=[jax.ShapeDtypeStruct((s_len, D), F32), jax.ShapeDtypeStruct((s_len, D), BF16),
                             jax.ShapeDtypeStruct((2, D), F32), jax.ShapeDtypeStruct((1, D), F32),
                             jax.ShapeDtypeStruct((1, D), F32)])(x1, dh2, dx2, ao, nw2, ss2, g1)


def _mod1_bwd(ctx, x, dh, dx1, nw1, ss1, carry=None):
    s_len = dx1.shape[0]
    tt = L + s_len

    def body(c_ref, x_ref, dh_ref, dx1_ref, nw_ref, ss_ref, dx_ref, dss_ref, dnw_ref):
        i = pl.program_id(0)
        tok = jnp.where(i == 0, c_ref[...], x_ref[...])

        @pl.when(i == 0)
        def _():
            dnw_ref[...] = jnp.zeros_like(dnw_ref)

        @pl.when(i <= 1)
        def _():
            dss_ref[...] = jnp.zeros_like(dss_ref)

        dh_ = dh_ref[...]
        nw = nw_ref[...]
        scale = ss_ref[0][1:2]
        dxn, xh = _norm_bwd_rows(tok, dh_, nw, scale)

        @pl.when(i >= 1)
        def _():
            dx_ref[...] = dx1_ref[...] + dxn

        dsh = jnp.sum(dh_, axis=0, keepdims=True)
        dsc = jnp.sum(dh_ * xh * nw, axis=0, keepdims=True)
        dss_ref[...] += jnp.concatenate([dsh, dsc], axis=0)[None]
        dnw_ref[...] += jnp.sum(dh_ * xh * (1.0 + scale), axis=0, keepdims=True)

    row = pl.BlockSpec((TM, D), lambda i: (i, 0))
    lat = pl.BlockSpec((TM, D), lambda i: (jnp.maximum(i - 1, 0), 0))
    sel = pl.BlockSpec((1, 2, D), lambda i: (jnp.minimum(i, 1), 0, 0))
    return _pcall(body, name="mod1_bwd", grid=(tt // TM,),
                  in_specs=_tok_specs() + [row, lat, _full((1, D)), sel],
                  out_specs=[lat, sel, _full((1, D))],
                  out_shape=[jax.ShapeDtypeStruct((s_len, D), F32), jax.ShapeDtypeStruct((2, 2, D), F32),
                             jax.ShapeDtypeStruct((1, D), F32)], carry=carry)(ctx, x, dh, dx1, nw1, ss1)


def _rows(c):
    return slice(c * CH, (c + 1) * CH)


def _chunk_masks(rev, transpose=False):
    r = lax.broadcasted_iota(jnp.int32, (TM, TM), 0)
    c = lax.broadcasted_iota(jnp.int32, (TM, TM), 1)
    same = (r // CH) == (c // CH)
    before = (c >= r) if (rev != transpose) else (c <= r)
    return same & before, same


def _chunk_scan(x, rev, transpose=False):
    r = lax.broadcasted_iota(jnp.int32, (CH, CH), 0)
    c = lax.broadcasted_iota(jnp.int32, (CH, CH), 1)
    tri = ((c >= r) if (rev != transpose) else (c <= r)).astype(F32)
    return jnp.concatenate([_dot(tri, x[_rows(ch)], prec=HI) for ch in range(x.shape[0] // CH)], axis=0)


def _chunk_total(x):
    return jnp.concatenate([jnp.broadcast_to(jnp.sum(x[_rows(ch)], axis=0, keepdims=True), (CH, x.shape[1]))
                            for ch in range(x.shape[0] // CH)], axis=0)


def _hgrn_gate(fl, qraw, lg):
    lb = 1.0 / (1.0 + jnp.exp(lg[1:2] - lg[0:1]))
    sg = _sig(fl)
    f = lb + (1.0 - lb) * sg
    q = qraw * _sig(qraw) * (HGD ** -0.5)
    return lb, sg, f, q


def _hgrn_fwd(p, lg, *, rev, carry=None, readout=None):
    tt = p.shape[0]
    nt = tt // TM
    ncht = TM // CH
    d = 1 if rev else 0

    def tile_of(s):
        return jnp.where(s == 0, 0, nt - s) if rev else s

    def body(*refs):
        if readout is None:
            f_ref, inp_ref, q_ref, lg_ref, o_ref, st_ref, state = refs
        else:
            f_ref, inp_ref, q_ref, lg_ref, oo_ref, g_ref, hw_ref, o_ref, st_ref, y_ref, state = refs
        s = pl.program_id(0)

        @pl.when(s == 0)
        def _():
            state[...] = jnp.zeros_like(state)

        _, _, f, q = _hgrn_gate(f_ref[...], q_ref[...], lg_ref[0])
        lf = jnp.log(f)
        causal, _ = _chunk_masks(rev)
        cum = _chunk_scan(lf, rev)
        tot = _chunk_total(lf)
        qd = (q * jnp.exp(cum)).astype(BF16)
        kd = ((1.0 - f) * jnp.exp(-cum)).astype(BF16)
        ke = ((1.0 - f) * jnp.exp(tot - cum)).astype(BF16)
        et = jnp.exp(tot)
        v = inp_ref[...].astype(BF16)
        order = range(ncht - 1, -1, -1) if rev else range(ncht)
        outs = []
        for h in range(4):
            sl = slice(h * HGD, (h + 1) * HGD)
            qd_, kd_, ke_, v_ = qd[:, sl], kd[:, sl], ke[:, sl], v[:, sl]
            pm = jnp.where(causal, _dot(qd_, kd_, NT), 0.0).astype(BF16)
            o_h = _dot(pm, v_)
            upd = [_dot(v_[_rows(c)], ke_[_rows(c)], TN) for c in range(ncht)]
            st = state[h]
            for c in order:
                st_ref[c, h] = st
                st = st * et[c * CH:c * CH + 1, sl] + upd[c]
            state[h] = st
            inter = [_dot(qd_[_rows(c)], st_ref[c, h].astype(BF16), NT) for c in range(ncht)]
            outs.append(o_h + jnp.concatenate(inter, axis=0))
        o_tile = jnp.concatenate(outs, axis=1)
        o_ref[...] = o_tile
        if readout is not None:
            @pl.when(tile_of(s) >= 1)
            def _():
                g = g_ref[...]
                y_ref[...] = (_head_rms(oo_ref[...] + o_tile, None, 4) * hw_ref[...] * (g * _sig(g))).astype(BF16)

    def col(cb):
        return pl.BlockSpec((TM, HGW), lambda s: (tile_of(s), cb))

    in_specs = [col(C_FB if rev else C_FF), col(C_INP), col(C_QHG), pl.BlockSpec((1, 2, HGW), lambda s: (d, 0, 0))]
    out_specs = [col(0), pl.BlockSpec((ncht, 4, HGD, HGD), lambda s: (tile_of(s), 0, 0, 0))]
    out_shape = [jax.ShapeDtypeStruct((tt, HGW), F32), jax.ShapeDtypeStruct((nt * ncht, 4, HGD, HGD), F32)]
    args = [p, p, p, lg]
    if readout is not None:
        in_specs += [col(0), col(C_GHG), _full((1, HGW))]
        args += [readout[0], p, readout[1]]
        assert rev
        out_specs.append(pl.BlockSpec((TM, HGW), lambda s: (jnp.where(s == 0, nt - 2, tile_of(s) - 1), 0)))
        out_shape.append(jax.ShapeDtypeStruct((tt - L, HGW), BF16))
    return _pcall(body, name="hgrn_fwd_rev" if rev else "hgrn_fwd", grid=(nt,), in_specs=in_specs,
                  out_specs=out_specs, out_shape=out_shape, scratch=[pltpu.VMEM((4, HGD, HGD), F32)],
                  carry=carry)(*args)


def _hgrn_bwd(p, lg, do, st, dp, prev, *, rev, carry=None):
    tt = p.shape[0]
    nt = tt // TM
    ncht = TM // CH
    d = 1 if rev else 0
    second = prev is not None

    def tile_of(s):
        return jnp.where(s == nt - 1, 0, s + 1) if rev else nt - 1 - s

    def body(*refs):
        if second:
            (f_ref, inp_ref, q_ref, lg_ref, do_ref, st_ref, dvp_ref, dqp_ref, _dp_in,
             dp_ref, dlg_ref, dstate) = refs
        else:
            (f_ref, inp_ref, q_ref, lg_ref, do_ref, st_ref, _dp_in,
             dp_ref, dv_ref, dq_ref, dlg_ref, dstate) = refs
        s = pl.program_id(0)
        tile = tile_of(s)

        @pl.when(s == 0)
        def _():
            dstate[...] = jnp.zeros_like(dstate)
            dlg_ref[...] = jnp.zeros_like(dlg_ref)

        qraw = q_ref[...]
        lb, sg, f, q = _hgrn_gate(f_ref[...], qraw, lg_ref[0])
        lf = jnp.log(f)
        causal, _ = _chunk_masks(rev)
        causal_t, _ = _chunk_masks(rev, transpose=True)
        cum = _chunk_scan(lf, rev)
        tot = _chunk_total(lf)
        ea, eb, ee, et = jnp.exp(cum), jnp.exp(-cum), jnp.exp(tot - cum), jnp.exp(tot)
        qdf, kdf, kef = q * ea, (1.0 - f) * eb, (1.0 - f) * ee
        qd, kd, ke = qdf.astype(BF16), kdf.astype(BF16), kef.astype(BF16)
        v = inp_ref[...].astype(BF16)
        dob = jnp.where(tile == 0, 0.0, do_ref[...]).astype(BF16)
        order = range(ncht) if rev else range(ncht - 1, -1, -1)
        dq_l, dk_l, dv_l, dcum_l, dtot_l = [], [], [], [], []
        for h in range(4):
            sl = slice(h * HGD, (h + 1) * HGD)
            qd_, kd_, ke_, v_, do_ = qd[:, sl], kd[:, sl], ke[:, sl], v[:, sl], dob[:, sl]
            pmt = jnp.where(causal_t, _dot(kd_, qd_, NT), 0.0).astype(BF16)
            dpm = jnp.where(causal, _dot(do_, v_, NT), 0.0).astype(BF16)
            dpmt = jnp.where(causal_t, _dot(v_, do_, NT), 0.0).astype(BF16)
            dv = _dot(pmt, do_)
            dqd = _dot(dpm, kd_)
            dkd = _dot(dpmt, qd_)
            upd = [_dot(do_[_rows(c)], qd_[_rows(c)], TN) for c in range(ncht)]
            ds = dstate[h]
            ds1 = [None] * ncht
            for c in order:
                ds1[c] = ds
                ds = ds * et[c * CH:c * CH + 1, sl] + upd[c]
            dstate[h] = ds
            dke_c, dv_c, dqd_c, dtot_c = [], [], [], []
            for c in range(ncht):
                st0 = st_ref[c, h]
                dsb = ds1[c].astype(BF16)
                dke_ = _dot(v_[_rows(c)], dsb)
                dke_c.append(dke_)
                dv_c.append(_dot(ke_[_rows(c)], dsb, NT))
                dqd_c.append(_dot(do_[_rows(c)], st0.astype(BF16)))
                dt = (jnp.sum(ds1[c] * st0, axis=0, keepdims=True) * et[c * CH:c * CH + 1, sl]
                      + jnp.sum(dke_ * kef[_rows(c), sl], axis=0, keepdims=True))
                dtot_c.append(jnp.broadcast_to(dt, (CH, HGD)))
            dke = jnp.concatenate(dke_c, axis=0)
            dqd = dqd + jnp.concatenate(dqd_c, axis=0)
            dv_l.append(dv + jnp.concatenate(dv_c, axis=0))
            dtot_l.append(jnp.concatenate(dtot_c, axis=0))
            dq_l.append(dqd * ea[:, sl])
            dk_l.append(dkd * eb[:, sl] + dke * ee[:, sl])
            dcum_l.append(dqd * qdf[:, sl] - dkd * kdf[:, sl] - dke * kef[:, sl])
        dcum = jnp.concatenate(dcum_l, axis=1)
        dlf = _chunk_scan(dcum, rev, transpose=True) + jnp.concatenate(dtot_l, axis=1)
        dq_t = jnp.concatenate(dq_l, axis=1)
        dv_t = jnp.concatenate(dv_l, axis=1)

        df = dlf / f - jnp.concatenate(dk_l, axis=1)
        dfl = df * (1.0 - lb) * sg * (1.0 - sg)
        dlb = jnp.sum(df * (1.0 - sg), axis=0, keepdims=True)
        dl0 = dlb * lb * (1.0 - lb)
        dlg_ref[...] += jnp.concatenate([dl0, -dl0], axis=0)[None]
        if second:
            sq = _sig(qraw)
            dqr = (dqp_ref[...] + dq_t) * (HGD ** -0.5) * (sq * (1.0 + qraw * (1.0 - sq)))
            dp_ref[...] = jnp.concatenate([dfl, dvp_ref[...] + dv_t, dqr], axis=1).astype(BF16)
        else:
            dp_ref[...] = dfl.astype(BF16)
            dv_ref[...] = dv_t
            dq_ref[...] = dq_t

    def col(cb):
        return pl.BlockSpec((TM, HGW), lambda s: (tile_of(s), cb))

    tok = pl.BlockSpec((TM, HGW), lambda s: (tile_of(s), 0))
    in_specs = [col(C_FB if rev else C_FF), col(C_INP), col(C_QHG),
                pl.BlockSpec((1, 2, HGW), lambda s: (d, 0, 0)),
                pl.BlockSpec((TM, HGW), lambda s: (jnp.maximum(tile_of(s) - 1, 0), 0)),
                pl.BlockSpec((ncht, 4, HGD, HGD), lambda s: (tile_of(s), 0, 0, 0))]
    args = [p, p, p, lg, do, st]
    dlg_spec = _full((1, 2, HGW))
    dlg_shape = jax.ShapeDtypeStruct((1, 2, HGW), F32)
    if second:
        in_specs += [tok, tok]
        args += [prev[0], prev[1]]
        out_specs = [pl.BlockSpec((TM, 3 * HGW), lambda s: (tile_of(s), 0)), dlg_spec]
        out_shape = [jax.ShapeDtypeStruct(dp.shape, BF16), dlg_shape]
    else:
        out_specs = [pl.BlockSpec((TM, HGW), lambda s: (tile_of(s), C_FB if rev else C_FF)), tok, tok, dlg_spec]
        out_shape = [jax.ShapeDtypeStruct(dp.shape, BF16), jax.ShapeDtypeStruct((tt, HGW), F32),
                     jax.ShapeDtypeStruct((tt, HGW), F32), dlg_shape]
    in_specs.append(ANY)
    args.append(dp)
    return _pcall(body, name="hgrn_bwd_rev" if rev else "hgrn_bwd", grid=(nt,),
                  in_specs=in_specs, out_specs=out_specs, out_shape=out_shape,
                  scratch=[pltpu.VMEM((4, HGD, HGD), F32)],
                  aliases={len(args) - 1: 0}, carry=carry)(*args)


def _head_rms(o, w, nheads):
    outs = []
    for h in range(nheads):
        oh = o[:, h * HGD:(h + 1) * HGD]
        outs.append(oh * lax.rsqrt(jnp.mean(oh * oh, axis=-1, keepdims=True) + EPS))
    return jnp.concatenate(outs, axis=1)


def _readout_bwd(o0, o1, p, hw4, dy, dp, carry=None):
    tt = o0.shape[0]
    s_len = tt - L

    def body(o0_ref, o1_ref, g_ref, w_ref, dy_ref, _dp_in, dp_ref, do_ref, dw_ref):
        i = pl.program_id(0)

        @pl.when(i == 0)
        def _():
            dw_ref[...] = jnp.zeros_like(dw_ref)
            dp_ref[...] = jnp.zeros_like(dp_ref)

        @pl.when(i >= 1)
        def _():
            o = o0_ref[...] + o1_ref[...]
            g = g_ref[...]
            w = w_ref[...]
            sg = _sig(g)
            dy_ = dy_ref[...]
            dsw = dy_ * (g * sg)
            outs, xhs = [], []
            for h in range(4):
                sl = slice(h * HGD, (h + 1) * HGD)
                oh = o[:, sl]
                r = lax.rsqrt(jnp.mean(oh * oh, axis=-1, keepdims=True) + EPS)
                xh = oh * r
                dxh = dsw[:, sl] * w[:, sl]
                outs.append(r * (dxh - xh * jnp.mean(dxh * xh, axis=-1, keepdims=True)))
                xhs.append(xh)
            xh = jnp.concatenate(xhs, axis=1)
            do_ref[...] = jnp.concatenate(outs, axis=1)
            dp_ref[...] = (dy_ * xh * w * (sg * (1.0 + g * (1.0 - sg)))).astype(BF16)
            dw_ref[...] += jnp.sum(dsw * xh, axis=0, keepdims=True)

    tok = pl.BlockSpec((TM, HGW), lambda i: (i, 0))
    lat = pl.BlockSpec((TM, HGW), lambda i: (jnp.maximum(i - 1, 0), 0))
    return _pcall(body, name="readout_bwd", grid=(tt // TM,),
                  in_specs=[tok, tok, pl.BlockSpec((TM, HGW), lambda i: (i, C_GHG)), _full((1, HGW)), lat, ANY],
                  out_specs=[pl.BlockSpec((TM, HGW), lambda i: (i, C_GHG)), lat, _full((1, HGW))],
                  out_shape=[jax.ShapeDtypeStruct(dp.shape, BF16), jax.ShapeDtypeStruct((s_len, HGW), F32),
                             jax.ShapeDtypeStruct((1, HGW), F32)],
                  aliases={5: 0}, carry=carry)(o0, o1, p, hw4, dy, dp)


def _rope_tables(s_len):
    t = np.arange(s_len)
    inv = ROPE_THETA ** (-np.arange(0, 32, 2, dtype=np.float64) / 32)
    def half(pos):
        ang = pos[:, None].astype(np.float64) * inv[None, :]
        return (np.concatenate([np.cos(ang), np.cos(ang)], 1), np.concatenate([-np.sin(ang), np.sin(ang)], 1))
    cr, sr = half(t // GRID_W)
    cc, sc = half(t % GRID_W)
    cos = np.concatenate([cr, cc, cr, cc], 1)
    sin = np.concatenate([sr, sc, sr, sc], 1)
    cos = np.concatenate([np.ones((L, 128)), cos], 0)
    sin = np.concatenate([np.zeros((L, 128)), sin], 0)
    return jnp.asarray(cos, F32), jnp.asarray(sin, F32)


def _blockdiag(n, w):
    i = np.arange(n)
    return jnp.asarray((i[:, None] // w == i[None, :] // w) / float(w), F32)


def _dup_matrix():
    m = np.zeros((128, 512), np.float32)
    for g in range(2):
        for j in range(4):
            for dd in range(HDIM):
                m[64 * g + dd, 256 * g + 64 * j + dd] = 1.0
    return m


def _head_mean(x, blockdiag):
    return _dot(x, blockdiag, prec=lax.Precision.HIGH)


def _rot(x):
    n = x.shape[1]
    lane = lax.broadcasted_iota(jnp.int32, x.shape, 1)
    return jnp.where((lane % 32) < 16, pltpu.roll(x, n - 16, 1), pltpu.roll(x, 16, 1))


def _qk_prep(p, cos, sin, qnw8, knw2, bd512, bd128, dup):
    tt = p.shape[0]

    def body(q_ref, kv_ref, cos_ref, sin_ref, qw_ref, kw_ref, b5_ref, b1_ref, dup_ref,
             qr_ref, k4_ref, v4_ref):
        cos_, sin_ = cos_ref[...], sin_ref[...]
        q = q_ref[...]
        qn = q * lax.rsqrt(_head_mean(q * q, b5_ref[...]) + EPS) * qw_ref[...]
        cos4 = jnp.concatenate([cos_] * 4, axis=1)
        sin4 = jnp.concatenate([sin_] * 4, axis=1)
        qr_ref[...] = ((qn * cos4 + _rot(qn) * sin4) * (HDIM ** -0.5)).astype(BF16)
        kv = kv_ref[...]
        k, v = kv[:, :128], kv[:, 128:]
        kn = k * lax.rsqrt(_head_mean(k * k, b1_ref[...]) + EPS) * kw_ref[...]
        kr = kn * cos_ + _rot(kn) * sin_
        k4_ref[...] = _bdot(kr, dup_ref[...]).astype(BF16)
        v4_ref[...] = _bdot(v, dup_ref[...]).astype(BF16)

    row = lambda w, cb: pl.BlockSpec((TM, w), lambda i: (i, cb))
    out = jax.ShapeDtypeStruct((tt, ATW), BF16)
    return _pcall(body, name="qk_prep", grid=(tt // TM,),
                  in_specs=[row(ATW, C_QRAW), row(256, C_KV), row(128, 0), row(128, 0),
                            _full((1, ATW)), _full((1, 128)), _full((ATW, ATW)), _full((128, 128)),
                            _full((128, ATW))],
                  out_specs=[row(ATW, 0)] * 3, out_shape=[out] * 3)(
                      p, p, cos, sin, qnw8, knw2, bd512, bd128, dup)


def _attn_masks(i, nb):
    r = lax.broadcasted_iota(jnp.int32, (4 * BLK, 3 * BLK + L), 0) % BLK
    c = lax.broadcasted_iota(jnp.int32, (4 * BLK, 3 * BLK + L), 1)
    kpos = (i - 1) * BLK + c
    loc = (jnp.abs(c - BLK - r) <= BLK) & (kpos >= 0) & (kpos < nb * BLK)
    return loc | (c >= 3 * BLK)


def _stack_mask():
    r = lax.broadcasted_iota(jnp.int32, (4 * BLK, 256), 0)
    lane = lax.broadcasted_iota(jnp.int32, (4 * BLK, 256), 1)
    return (r // BLK) == (lane // HDIM)


def _stack_heads(xg, fill=0.0):
    x4 = jnp.concatenate([xg] * 4, axis=0)
    return jnp.where(_stack_mask(), x4, jnp.full_like(x4, fill))


def _unstack_heads(x4):
    out = jnp.where(_lane_mask(0), x4[0:BLK], 0.0)
    for j in range(1, 4):
        out = out + jnp.where(_lane_mask(j), x4[j * BLK:(j + 1) * BLK], 0.0)
    return out


def _per_head_rows(vals):
    return jnp.concatenate([jnp.broadcast_to(v, (BLK, 1)) for v in vals], axis=0)


def _lane_mask(j):
    lane = lax.broadcasted_iota(jnp.int32, (1, 256), 1)
    return (lane // HDIM) == j


def _attn_specs(nb):
    blk = lambda off: pl.BlockSpec((BLK, ATW), lambda i: (jnp.clip(i + off, 0, nb - 1) + 2, 0))
    ctx = pl.BlockSpec((L, ATW), lambda i: (0, 0))
    return blk, ctx


def _attn_fwd(qr, k4, v4, sinks, carry=None):
    tt = qr.shape[0]
    s_len = tt - L
    nb = s_len // BLK

    def body(sk_ref, q_ref, kp, ko, kn, kc, vp, vo, vn, vc, y_ref, lse_ref):
        i = pl.program_id(0)
        valid = _attn_masks(i, nb)
        q = q_ref[...]
        ys, lses = [], []
        for g in range(2):
            gs = slice(256 * g, 256 * g + 256)
            kcat = jnp.concatenate([kp[:, gs], ko[:, gs], kn[:, gs], kc[:, gs]], axis=0)
            vcat = jnp.concatenate([vp[:, gs], vo[:, gs], vn[:, gs], vc[:, gs]], axis=0)
            sink4 = _per_head_rows([sk_ref[4 * g + j] for j in range(4)])
            q4 = _stack_heads(q[:, gs])
            o_parts, l_parts = [], []
            for hp in range(2):
                rows = slice(2 * BLK * hp, 2 * BLK * (hp + 1))
                sink = sink4[rows]
                s = jnp.where(valid[rows], _dot(q4[rows], kcat, NT), -1e30)
                m = jnp.maximum(jnp.max(s, axis=-1, keepdims=True), sink)
                e = jnp.exp(s - m)
                den = jnp.sum(e, axis=-1, keepdims=True) + jnp.exp(sink - m)
                o_parts.append(_bdot(e * (1.0 / den), vcat))
                l_parts.append(jnp.broadcast_to(m + jnp.log(den), (2 * BLK, 256)))
            ys.append(_unstack_heads(jnp.concatenate(o_parts, axis=0)))
            lses.append(_unstack_heads(jnp.concatenate(l_parts, axis=0)))
        y_ref[...] = jnp.concatenate(ys, axis=1).astype(BF16)
        lse_ref[...] = jnp.concatenate(lses, axis=1)

    blk, ctx = _attn_specs(nb)
    out = pl.BlockSpec((BLK, ATW), lambda i: (i, 0))
    return _pcall(body, name="attn_fwd", grid=(nb,),
                  in_specs=[pl.BlockSpec(memory_space=pltpu.SMEM), blk(0),
                            blk(-1), blk(0), blk(1), ctx, blk(-1), blk(0), blk(1), ctx],
                  out_specs=[out, out],
                  out_shape=[jax.ShapeDtypeStruct((s_len, ATW), BF16),
                             jax.ShapeDtypeStruct((s_len, ATW), F32)], carry=carry)(
                      sinks, qr, k4, k4, k4, k4, v4, v4, v4, v4)


def _attn_bwd(qr, k4, v4, sinks, y, lse, dy, carry=None):
    tt = qr.shape[0]
    s_len = tt - L
    nb = s_len // BLK

    def body(sk_ref, q_ref, kp, ko, kn, kc, vp, vo, vn, vc, y_ref, lse_ref, dy_ref,
             dq_ref, dkw_ref, dvw_ref, dkc_ref, dvc_ref, dsk_ref):
        i = pl.program_id(0)

        @pl.when(i == 0)
        def _():
            dkc_ref[...] = jnp.zeros_like(dkc_ref)
            dvc_ref[...] = jnp.zeros_like(dvc_ref)
            dsk_ref[...] = jnp.zeros_like(dsk_ref)

        valid = _attn_masks(i, nb)
        q = q_ref[...]
        dy_ = dy_ref[...]
        dly = dy_ * y_ref[...].astype(F32)
        lse_ = lse_ref[...]
        dqs = []
        for g in range(2):
            gs = slice(256 * g, 256 * g + 256)
            kcat = jnp.concatenate([kp[:, gs], ko[:, gs], kn[:, gs], kc[:, gs]], axis=0)
            vcat = jnp.concatenate([vp[:, gs], vo[:, gs], vn[:, gs], vc[:, gs]], axis=0)
            q4 = _stack_heads(q[:, gs])
            dy4 = _stack_heads(dy_[:, gs]).astype(BF16)
            lse4 = jnp.max(_stack_heads(lse_[:, gs], fill=-1e30), axis=-1, keepdims=True)
            delta = jnp.sum(_stack_heads(dly[:, gs]), axis=-1, keepdims=True)
            sink = _per_head_rows([sk_ref[4 * g + j] for j in range(4)])
            pr = jnp.where(valid, jnp.exp(_dot(q4, kcat, NT) - lse4), 0.0)
            dsb = (pr * (_dot(dy4, vcat, NT) - delta)).astype(BF16)
            dsink = jnp.exp(sink - lse4) * delta
            for j in range(4):
                dsk_ref[4 * g + j:4 * g + j + 1, :] += jnp.broadcast_to(
                    -jnp.sum(dsink[j * BLK:(j + 1) * BLK], axis=0, keepdims=True), (1, 128))
            dqs.append(_unstack_heads(_dot(dsb, kcat)))
            dkg = _dot(dsb, q4, TN)
            dvg = _dot(pr.astype(BF16), dy4, TN)
            dkw_ref[0, :, gs] = dkg[:3 * BLK]
            dvw_ref[0, :, gs] = dvg[:3 * BLK]
            dkc_ref[:, gs] += dkg[3 * BLK:]
            dvc_ref[:, gs] += dvg[3 * BLK:]
        dq_ref[...] = jnp.concatenate(dqs, axis=1)

    blk, ctx = _attn_specs(nb)
    out = pl.BlockSpec((BLK, ATW), lambda i: (i, 0))
    win = pl.BlockSpec((1, 3 * BLK, ATW), lambda i: (i, 0, 0))
    acc = _full((L, ATW))
    return _pcall(body, name="attn_bwd", grid=(nb,),
                  in_specs=[pl.BlockSpec(memory_space=pltpu.SMEM), blk(0),
                            blk(-1), blk(0), blk(1), ctx, blk(-1), blk(0), blk(1), ctx, out, out, out],
                  out_specs=[out, win, win, acc, acc, _full((8, 128))],
                  out_shape=[jax.ShapeDtypeStruct((s_len, ATW), F32),
                             jax.ShapeDtypeStruct((nb, 3 * BLK, ATW), F32),
                             jax.ShapeDtypeStruct((nb, 3 * BLK, ATW), F32),
                             jax.ShapeDtypeStruct((L, ATW), F32), jax.ShapeDtypeStruct((L, ATW), F32),
                             jax.ShapeDtypeStruct((8, 128), F32)], carry=carry)(
                      sinks, qr, k4, k4, k4, k4, v4, v4, v4, v4, y, lse, dy)


def _attn_post(p, cos, sin, qnw8, knw2, bd512, bd128, dupt, dq, dkw, dvw, dkc, dvc, dp, carry=None):
    tt = p.shape[0]
    s_len = tt - L
    nb = s_len // BLK
    nctx = L // BLK

    def body(q_ref, kv_ref, cos_ref, sin_ref, qw_ref, kw_ref, b5_ref, b1_ref, dupt_ref,
             dq_ref, kwp, kwo, kwn, vwp, vwo, vwn, dkc_ref, dvc_ref, _dp_in,
             dp_ref, dqw_ref, dkw_ref):
        t = pl.program_id(0)
        j = t - nctx

        @pl.when(t == 0)
        def _():
            dqw_ref[...] = jnp.zeros_like(dqw_ref)
            dkw_ref[...] = jnp.zeros_like(dkw_ref)

        is_lat = t >= nctx
        cos_, sin_ = cos_ref[...], sin_ref[...]
        has_p = is_lat & (j >= 1)
        has_n = is_lat & (j <= nb - 2)
        dk4 = (jnp.where(is_lat, kwo[0], dkc_ref[...]) + jnp.where(has_p, kwp[0], 0.0)
               + jnp.where(has_n, kwn[0], 0.0))
        dv4 = (jnp.where(is_lat, vwo[0], dvc_ref[...]) + jnp.where(has_p, vwp[0], 0.0)
               + jnp.where(has_n, vwn[0], 0.0))
        dkr = _dot(dk4, dupt_ref[...], prec=HI)
        dv = _dot(dv4, dupt_ref[...], prec=HI)
        kv = kv_ref[...]
        k = kv[:, :128]
        kw = kw_ref[...]
        rk = lax.rsqrt(_head_mean(k * k, b1_ref[...]) + EPS)
        xk = k * rk
        dkn = dkr * cos_ + _rot(dkr * sin_)
        dxk = dkn * kw
        dk = rk * (dxk - xk * _head_mean(dxk * xk, b1_ref[...]))
        dkw_ref[...] += jnp.sum(dkn * xk, axis=0, keepdims=True)
        q = q_ref[...]
        qw = qw_ref[...]
        rq = lax.rsqrt(_head_mean(q * q, b5_ref[...]) + EPS)
        xq = q * rq
        cos4 = jnp.concatenate([cos_] * 4, axis=1)
        sin4 = jnp.concatenate([sin_] * 4, axis=1)
        dqr = jnp.where(is_lat, dq_ref[...], 0.0) * (HDIM ** -0.5)
        dqn = dqr * cos4 + _rot(dqr * sin4)
        dxq = dqn * qw
        dqraw = rq * (dxq - xq * _head_mean(dxq * xq, b5_ref[...]))
        dqw_ref[...] += jnp.sum(dqn * xq, axis=0, keepdims=True)
        dp_ref[...] = jnp.concatenate([dqraw, dk, dv], axis=1).astype(BF16)

    row = lambda w, cb: pl.BlockSpec((BLK, w), lambda t: (t, cb))
    lat = pl.BlockSpec((BLK, ATW), lambda t: (jnp.maximum(t - nctx, 0), 0))

    def part(off):
        return pl.BlockSpec((1, BLK, ATW), lambda t: (jnp.clip(t - nctx + off, 0, nb - 1), 1 - off, 0))

    cacc = pl.BlockSpec((BLK, ATW), lambda t: (jnp.minimum(t, nctx - 1), 0))
    return _pcall(body, name="attn_post", grid=(tt // BLK,),
                  in_specs=[row(ATW, C_QRAW), row(256, C_KV), row(128, 0), row(128, 0),
                            _full((1, ATW)), _full((1, 128)), _full((ATW, ATW)), _full((128, 128)),
                            _full((ATW, 128)), lat, part(-1), part(0), part(1), part(-1), part(0), part(1),
                            cacc, cacc, ANY],
                  out_specs=[pl.BlockSpec((BLK, 768), lambda t: (t, C_QKV)), _full((1, ATW)), _full((1, 128))],
                  out_shape=[jax.ShapeDtypeStruct(dp.shape, BF16), jax.ShapeDtypeStruct((1, ATW), F32),
                             jax.ShapeDtypeStruct((1, 128), F32)],
                  aliases={18: 0}, carry=carry)(p, p, cos, sin, qnw8, knw2, bd512, bd128, dupt,
                                   dq, dkw, dkw, dkw, dvw, dvw, dvw, dkc, dvc, dp)


def _branch_merge(y_hg, y_at, bh4, ba4, p):
    s_len = y_hg.shape[0]

    def body(yh_ref, ya_ref, bh_ref, ba_ref, gh_ref, ga_ref, ah_ref, aa_ref, m_ref):
        yh, ya = yh_ref[...], ya_ref[...]
        ah = jnp.concatenate([_bdot(yh, bh_ref[j]) for j in range(4)], axis=1)
        aa = jnp.concatenate([_bdot(ya, ba_ref[j]) for j in range(4)], axis=1)
        ah_ref[...] = ah.astype(BF16)
        aa_ref[...] = aa.astype(BF16)
        m_ref[...] = (_sig(gh_ref[...]) * ah + _sig(ga_ref[...]) * aa).astype(BF16)

    row = pl.BlockSpec((TM, D), lambda i: (i, 0))
    y = pl.BlockSpec((TM, HGW), lambda i: (i, 0))
    f = jax.ShapeDtypeStruct((s_len, D), BF16)
    return _pcall(body, name="branch_merge", grid=(s_len // TM,),
                  in_specs=[y, y, _full(bh4.shape), _full(ba4.shape),
                            pl.BlockSpec((TM, D), lambda i: (i + 1, 2)), pl.BlockSpec((TM, D), lambda i: (i + 1, 3))],
                  out_specs=[row, row, row],
                  out_shape=[f, f, jax.ShapeDtypeStruct((s_len, D), BF16)])(y_hg, y_at, bh4, ba4, p, p)


def _branch_bwd(dmh, dma, bh4, ba4, y_hg, y_at):
    s_len = dmh.shape[0]
    nk = s_len // TSD
    ns = D // 4

    def body(dh_ref, da_ref, bh_ref, ba_ref, yh_ref, ya_ref, dyh_ref, dya_ref, gh_ref, ga_ref, acc_h, acc_a):
        t = pl.program_id(0)

        @pl.when(t == 0)
        def _():
            acc_h[...] = jnp.zeros_like(acc_h)
            acc_a[...] = jnp.zeros_like(acc_a)

        for d_ref, w_ref, y_ref, dy_ref, acc in ((dh_ref, bh_ref, yh_ref, dyh_ref, acc_h),
                                                 (da_ref, ba_ref, ya_ref, dya_ref, acc_a)):
            y = y_ref[...]
            dy = jnp.zeros((TSD, HGW), F32)
            for j in range(4):
                dj = d_ref[:, j * ns:(j + 1) * ns]
                dy = dy + _bdot(dj, w_ref[j], NT)
                acc[j] += _bdot(y, dj, TN)
            dy_ref[...] = dy

        @pl.when(t == nk - 1)
        def _():
            gh_ref[...] = acc_h[...].astype(BF16)
            ga_ref[...] = acc_a[...].astype(BF16)

    dm = pl.BlockSpec((TSD, D), lambda t: (t, 0))
    y = pl.BlockSpec((TSD, HGW), lambda t: (t, 0))
    w = _full(bh4.shape)
    fy = jax.ShapeDtypeStruct((s_len, HGW), F32)
    gw = jax.ShapeDtypeStruct(bh4.shape, BF16)
    return _pcall(body, name="branch_bwd", grid=(nk,), in_specs=[dm, dm, w, w, y, y],
                  out_specs=[y, y, w, w], out_shape=[fy, fy, gw, gw],
                  scratch=[pltpu.VMEM(bh4.shape, F32)] * 2)(dmh, dma, bh4, ba4, y_hg, y_at)


def _merge_bwd(dattn, w_o, mixed, ah, aa, p, carry=None):
    tt = p.shape[0]
    s_len = tt - L
    nt = tt // TM

    def body(da_ref, wo_ref, mx_ref, ah_ref, aa_ref, gh_ref, ga_ref, dp_ref, dmh_ref, dma_ref, go_ref, acc):
        i = pl.program_id(0)

        @pl.when(i == 0)
        def _():
            dp_ref[...] = jnp.zeros_like(dp_ref)
            acc[...] = jnp.zeros_like(acc)

        @pl.when(i >= 1)
        def _():
            da = da_ref[...]
            acc[...] += _bdot(mx_ref[...], da, TN)
            dm_ = _bdot(da, wo_ref[...], NT)
            sh, sa = _sig(gh_ref[...]), _sig(ga_ref[...])
            dp_ref[...] = jnp.concatenate([dm_ * ah_ref[...].astype(F32) * sh * (1.0 - sh),
                                           dm_ * aa_ref[...].astype(F32) * sa * (1.0 - sa)], axis=1).astype(BF16)
            dmh_ref[...] = (dm_ * sh).astype(BF16)
            dma_ref[...] = (dm_ * sa).astype(BF16)

        @pl.when(i == nt - 1)
        def _():
            go_ref[...] = acc[...].astype(BF16)

    lat = pl.BlockSpec((TM, D), lambda i: (jnp.maximum(i - 1, 0), 0))
    return _pcall(body, name="merge_bwd", grid=(nt,),
                  in_specs=[lat, _full((D, D)), lat, lat, lat, pl.BlockSpec((TM, D), lambda i: (i, 2)),
                            pl.BlockSpec((TM, D), lambda i: (i, 3))],
                  out_specs=[pl.BlockSpec((TM, 2 * D), lambda i: (i, C_GATES)), lat, lat, _full((D, D))],
                  out_shape=[jax.ShapeDtypeStruct((tt, NCOL), BF16), jax.ShapeDtypeStruct((s_len, D), BF16),
                             jax.ShapeDtypeStruct((s_len, D), BF16), jax.ShapeDtypeStruct((D, D), BF16)],
                  scratch=[pltpu.VMEM((D, D), F32)], carry=carry)(dattn, w_o, mixed, ah, aa, p, p)


def _local_step(x, ctx, tgt, mod, modc, nw1, nw2, lg, hw, qnw, knw, sinks,
                w_in, wts, dist=None):
    s_len = x.shape[0]
    tt = s_len + L
    ss1 = jnp.stack([modc, mod[0:2]])
    ss2 = mod[3:5][None]
    g1, g2 = mod[2:3], mod[5:6]
    hw4 = jnp.tile(hw, (1, 4))
    qnw8 = jnp.tile(qnw, (1, 8))
    knw2 = jnp.tile(knw, (1, 2))
    cos, sin = _rope_tables(s_len)
    bd512, bd128 = _blockdiag(ATW, HDIM), _blockdiag(128, HDIM)
    dupm = _dup_matrix()
    dup, dupt = jnp.asarray(dupm, BF16), jnp.asarray(dupm.T, F32)
    tmt = tt

    def four(b):
        return b.reshape(4, 2 * b.shape[1], b.shape[2])

    def halves(g):
        return g.reshape(4, 2, g.shape[1] // 2, g.shape[2])

    h = _mod1(ctx, x, nw1, ss1)
    if dist is None:
        bh4, ba4, w_o, g4, u4, dn4 = wts
        p = _mm_in(h, w_in, tmt)
        o0, st0 = _hgrn_fwd(p, lg, rev=False)
        o1, st1, y_hg = _hgrn_fwd(p, lg, rev=True, readout=(o0, hw4))
    else:
        core, chip = dist
        half = wts[3].shape[1] // 2
        p, first = _mm_in(h, w_in, tmt, carry=_carry_join(_carry_gather(list(wts[0:3])),
                                                          _carry_gather([wts[3]], rows=[(0, half)])))
        (o0, st0), (g8,) = _hgrn_fwd(p, lg, rev=False, carry=_carry_gather([first[3]], rows=[(half, half)]))
        (o1, st1, y_hg), (dn8a,) = _hgrn_fwd(p, lg, rev=True, readout=(o0, hw4),
                                             carry=_carry_gather([wts[5]], rows=[(0, half)]))
        bh4, ba4, w_o, g4 = four(first[0]), four(first[1]), four(first[2]).reshape(D, D), four(g8)
    qr, k4, v4 = _qk_prep(p, cos, sin, qnw8, knw2, bd512, bd128, dup)
    if dist is None:
        y_at, lse = _attn_fwd(qr, k4, v4, sinks)
    else:
        (y_at, lse), (u8, dn8) = _attn_fwd(qr, k4, v4, sinks,
                                           carry=_carry_gather([wts[4], dn8a], rows=[None, (half, half)]))
        u4, dn4 = four(u8), four(dn8)
    ah, aa, mixed = _branch_merge(y_hg, y_at, bh4, ba4, p)
    ao, x1, h2 = _out_proj_mod2(mixed, w_o, x, g1, nw2, ss2)
    a4, b4, z4 = _ffn_up(h2, g4, u4)
    sq, dx2, dyb, dg2 = _ffn_down_loss(z4, dn4, x1, g2, tgt)

    da4, db4 = _ffn_dz(dyb, dn4, a4, b4)
    g_dn = _ffn_gdn(z4, dyb)
    if dist is None:
        dh2 = _ffn_dh2(da4, db4, g4, u4)
    else:
        dn_units = [halves(g_dn)]
        dh2, dn_recv = _ffn_dh2(da4, db4, g4, u4, carry=_carry_pairx(dn_units))
        dn_pairs = _rs_pair_add(dn_units, dn_recv, core)
    if dist is None:
        g_g, g_u = _ffn_ggu(h2, da4, db4)
    else:
        (g_g, g_u), c_dn = _ffn_ggu(h2, da4, db4, carry=_carry_chipx(dn_pairs))
        red_dn = _rs_chip_add(dn_pairs, c_dn, core, chip)
    dx1, dattn, dss2, dnw2, dg1 = _mod2_bwd(x1, dh2, dx2, ao, nw2, ss2, g1)
    if dist is None:
        dp, dmh, dma, g_o = _merge_bwd(dattn, w_o, mixed, ah, aa, p)
    else:
        gu_units = [halves(g_g), halves(g_u)]
        (dp, dmh, dma, g_o), gu_recv = _merge_bwd(dattn, w_o, mixed, ah, aa, p, carry=_carry_pairx(gu_units))
        ffn_pairs = list(dn_pairs) + list(_rs_pair_add(gu_units, gu_recv, core))
    dy_hg, dy_at, g_bh, g_ba = _branch_bwd(dmh, dma, bh4, ba4, y_hg, y_at)
    if dist is None:
        dp, do, dhw4 = _readout_bwd(o0, o1, p, hw4, dy_hg, dp)
        dq, dkw, dvw, dkc, dvc, dsk = _attn_bwd(qr, k4, v4, sinks, y_at, lse, dy_at)
        dp, dqnw8, dknw2 = _attn_post(p, cos, sin, qnw8, knw2, bd512, bd128, dupt, dq, dkw, dvw, dkc, dvc, dp)
    else:
        mix_units = [halves(g_bh), halves(g_ba), halves(g_o.reshape(4, D // 4, D))]
        (dp, do, dhw4), mix_recv = _readout_bwd(o0, o1, p, hw4, dy_hg, dp, carry=_carry_pairx(mix_units))
        mix_pairs = _rs_pair_add(mix_units, mix_recv, core)
        (dq, dkw, dvw, dkc, dvc, dsk), bwd = _attn_bwd(
            qr, k4, v4, sinks, y_at, lse, dy_at,
            carry=_carry_join(_carry_chipx(ffn_pairs[1:2]), _carry_sibx(red_dn)))
        red_g = _rs_chip_add(ffn_pairs[1:2], bwd[0:1], core, chip)
        (dp, dqnw8, dknw2), post = _attn_post(
            p, cos, sin, qnw8, knw2, bd512, bd128, dupt, dq, dkw, dvw, dkc, dvc, dp, carry=_carry_sibx(red_g))
    if dist is None:
        dp, dv0, dq0, dlg0 = _hgrn_bwd(p, lg, do, st0, dp, None, rev=False)
        dp, dlg1 = _hgrn_bwd(p, lg, do, st1, dp, (dv0, dq0), rev=True)
    else:
        (dp, dv0, dq0, dlg0), c_u = _hgrn_bwd(p, lg, do, st0, dp, None, rev=False,
                                              carry=_carry_chipx(ffn_pairs[2:3]))
        red_u = _rs_chip_add(ffn_pairs[2:3], c_u, core, chip)
        (dp, dlg1), last = _hgrn_bwd(p, lg, do, st1, dp, (dv0, dq0), rev=True,
                                     carry=_carry_join(_carry_chipx(mix_pairs), _carry_sibx(red_u)))
        mix_reds = _rs_chip_add(mix_pairs, last[0:3], core, chip)
        ffn_done = bwd[1:2] + post[0:1] + last[3:4]
    g_in = _mm_gin(dp, h, tmt)
    if dist is None:
        dh = _mm_dh(dp, w_in, tmt)
        gx, dss1, dnw1 = _mod1_bwd(ctx, x, dh, dx1, nw1, ss1)
        rs = None
    else:
        in_units = [halves(g_in.reshape(4, NCOL // 4, D))]
        dh, both = _mm_dh(dp, w_in, tmt, carry=_carry_join(_carry_pairx(in_units), _carry_sibx(mix_reds)))
        in_recv, mix_done = both[0:1], both[1:4]
        in_pairs = _rs_pair_add(in_units, in_recv, core)
        first_rows = (0, in_pairs[0].shape[1] // 2)
        (gx, dss1, dnw1), in_part = _mod1_bwd(ctx, x, dh, dx1, nw1, ss1,
                                              carry=_carry_chipx(in_pairs, rows=first_rows))
        rs = dict(ffn_done=ffn_done, mix_done=mix_done, in_pairs=in_pairs, in_part=in_part)

    dmod = jnp.concatenate([dss1[1], dg1, dss2, dg2], axis=0)
    dmodc = dss1[0]
    raw = (dss1, dg1, dss2, dg2, dnw1, dnw2, dhw4, dqnw8, dknw2, dsk, dlg0, dlg1)
    small = dict(raw=raw, dmod=dmod, dmodc=dmodc, dnw1=dnw1, dnw2=dnw2,
                 dhw=dhw4.reshape(4, HGD).sum(0, keepdims=True),
                 dqnw=dqnw8.reshape(8, HDIM).sum(0, keepdims=True),
                 dknw=dknw2.reshape(2, HDIM).sum(0, keepdims=True),
                 dsinks=dsk[:, 0], dlg=jnp.concatenate([dlg0, dlg1], axis=0))
    big = dict(w_in=g_in, w_bh=g_bh, w_ba=g_ba, w_o=g_o, w_g=g_g, w_u=g_u, w_dn=g_dn)
    return sq, gx, big, small, rs


def _place():
    x, y, c = lax.axis_index("x"), lax.axis_index("y"), lax.axis_index("c")
    return x, y, c


def _gather_blocks(x_refs, out_refs, send_sems, recv_sems, local_sems):
    n = len(out_refs)
    x, y, c = _place()
    me, sibling = (x, y, c), (x, y, 1 - c)
    chips = [(1 - x, y), (x, 1 - y), (1 - x, 1 - y)]

    def slot(u, px, py, pc):
        return out_refs[u].at[4 * px + 2 * py + pc]

    def copy(u, k, block, to, src=None):
        return pltpu.make_async_remote_copy(
            src_ref=slot(u, *block) if src is None else src, dst_ref=slot(u, *block),
            send_sem=send_sems.at[u, k], recv_sem=recv_sems.at[u, k], device_id=to, device_id_type=MESH)

    mines = [pltpu.make_async_copy(x_refs[u], slot(u, *me), local_sems.at[u]) for u in range(n)]
    for cp in mines:
        cp.start()
    first = []
    for u in range(n):
        first.append(copy(u, 0, me, sibling, src=x_refs[u]))
        first += [copy(u, 1 + j, me, (*chip, c), src=x_refs[u]) for j, chip in enumerate(chips)]
    for cp in first:
        cp.start()
    passed = []
    for j, chip in enumerate(chips):
        for u in range(n):
            copy(u, 1 + j, (*chip, c), me).wait_recv()
            fwd = copy(u, 4 + j, (*chip, c), sibling)
            fwd.start()
            passed.append(fwd)
    for u in range(n):
        copy(u, 0, sibling, me).wait_recv()
    for j, chip in enumerate(chips):
        for u in range(n):
            copy(u, 4 + j, (*chip, 1 - c), me).wait_recv()
    for cp in first + passed:
        cp.wait_send()
    for cp in mines:
        cp.wait()


def _gather_sems(n):
    return [pltpu.SemaphoreType.DMA((n, 7)), pltpu.SemaphoreType.DMA((n, 7)), pltpu.SemaphoreType.DMA((n,))]


def _allgather(blks, *, name):
    n = len(blks)
    vm = pl.BlockSpec(memory_space=pltpu.VMEM)

    def body(*refs):
        _peer_barrier("both")
        _gather_blocks(refs[:n], refs[n:2 * n], *refs[2 * n:])

    return pl.pallas_call(
        body, name=name, out_shape=[jax.ShapeDtypeStruct((8,) + b.shape, b.dtype) for b in blks],
        in_specs=[vm] * n, out_specs=[vm] * n, scratch_shapes=_gather_sems(n),
        compiler_params=pltpu.CompilerParams(collective_id=BARRIER_IDS["both"]))(*blks)


def _cast_place(ws, c, dev):
    n = len(ws)

    def body(s_ref, *refs):
        for u in range(n):
            refs[n + u][0] = refs[u][...].astype(BF16)

    in_specs, out_specs, out_shape = [], [], []
    for w in ws:
        q, cols = w.shape[0] // 4, w.shape[1]
        in_specs.append(pl.BlockSpec((q, cols), lambda i, s: (2 * s[0] + i, 0)))
        out_specs.append(pl.BlockSpec((1, q, cols), lambda i, s: (s[1], i, 0)))
        out_shape.append(jax.ShapeDtypeStruct((8, 2 * q, cols), BF16))
    return pl.pallas_call(
        body, name="cast_place",
        grid_spec=pltpu.PrefetchScalarGridSpec(num_scalar_prefetch=1, grid=(2,), in_specs=in_specs,
                                               out_specs=out_specs),
        out_shape=_out_hbm(out_shape),
        compiler_params=pltpu.CompilerParams(vmem_limit_bytes=48 << 20))(jnp.stack([c, dev]), *_in_hbm(ws))


def _gather_phases(out_refs, send_sems, recv_sems, rows=None):
    n = len(out_refs)
    x, y, c = _place()
    me, sibling = (x, y, c), (x, y, 1 - c)
    chips = [(1 - x, y), (x, 1 - y), (1 - x, 1 - y)]

    def copy(u, k, block, to):
        px, py, pc = block
        ref = out_refs[u].at[4 * px + 2 * py + pc]
        if rows is not None and rows[u] is not None:
            ref = ref.at[pl.ds(rows[u][0], rows[u][1])]
        return pltpu.make_async_remote_copy(src_ref=ref, dst_ref=ref, send_sem=send_sems.at[u, k],
                                            recv_sem=recv_sems.at[u, k], device_id=to, device_id_type=MESH)

    def start():
        for u in range(n):
            copy(u, 0, me, sibling).start()
            for j, chip in enumerate(chips):
                copy(u, 1 + j, me, (*chip, c)).start()

    def mid():
        for j, chip in enumerate(chips):
            for u in range(n):
                copy(u, 1 + j, (*chip, c), me).wait_recv()
                copy(u, 4 + j, (*chip, c), sibling).start()

    def end():
        for u in range(n):
            copy(u, 0, sibling, me).wait_recv()
        for j, chip in enumerate(chips):
            for u in range(n):
                copy(u, 4 + j, (*chip, 1 - c), me).wait_recv()
        for u in range(n):
            copy(u, 0, me, sibling).wait_send()
            for j, chip in enumerate(chips):
                copy(u, 1 + j, me, (*chip, c)).wait_send()
                copy(u, 4 + j, (*chip, c), sibling).wait_send()

    return start, mid, end


def _carry_gather(bufs, rows=None):
    n = len(bufs)
    return _Carry(bufs, [jax.ShapeDtypeStruct(b.shape, b.dtype) for b in bufs], {u: u for u in range(n)},
                  [pltpu.SemaphoreType.DMA((n, 7)), pltpu.SemaphoreType.DMA((n, 7))],
                  lambda ins, outs, sems: _gather_phases(outs, *sems, rows=rows), "both")


def _ag_small(raw, sq):
    def body(dss1, dg1, dss2, dg2, dnw1, dnw2, dhw4, dqnw8, dknw2, dsk, dlg0, dlg1, sq_ref,
             out_ref, tot_ref, blk, send_sems, recv_sems, local_sems):
        _peer_barrier("both")
        blk[...] = jnp.zeros_like(blk)
        blk[0:2, :] = dss1[1]
        blk[2:3, :] = dg1[...]
        blk[3:5, :] = dss2[...]
        blk[5:6, :] = dg2[...]
        blk[6:8, :] = dss1[0]
        blk[8:9, :] = dnw1[...]
        blk[9:10, :] = dnw2[...]
        blk[10:11, 0:HGW] = dhw4[...]
        blk[10:11, HGW:D] = dqnw8[...]
        blk[11:12, 0:128] = dknw2[...]
        blk[12:14, 0:HGW] = dlg0[0]
        blk[14:16, 0:HGW] = dlg1[0]
        blk[16:24, 0:128] = dsk[...]
        blk[24:25, :] = sq_ref[...]
        _gather_blocks([blk], [out_ref], send_sems, recv_sems, local_sems)
        acc = out_ref[0]
        for i in range(1, 8):
            acc = acc + out_ref[i]
        tot_ref[...] = acc

    vm = pl.BlockSpec(memory_space=pltpu.VMEM)
    return pl.pallas_call(
        body, name="ag_small",
        out_shape=[jax.ShapeDtypeStruct((8, 32, D), F32), jax.ShapeDtypeStruct((32, D), F32)],
        in_specs=[vm] * 13, out_specs=[vm, vm],
        scratch_shapes=[pltpu.VMEM((32, D), F32)] + _gather_sems(1),
        compiler_params=pltpu.CompilerParams(collective_id=BARRIER_IDS["both"]))(*raw, sq)


def _pairx_shapes(units):
    return [jax.ShapeDtypeStruct((4,) + g.shape[2:], g.dtype) for g in units]


def _pairx_phases(g_refs, r_refs, send_sems, recv_sems):
    n = len(g_refs)
    x, y, c = _place()
    cps = [pltpu.make_async_remote_copy(
        src_ref=g_refs[u].at[j, 1 - c], dst_ref=r_refs[u].at[j], send_sem=send_sems.at[u, j],
        recv_sem=recv_sems.at[u, j], device_id=(x, y, 1 - c), device_id_type=MESH)
        for u in range(n) for j in range(4)]

    def start():
        for cp in cps:
            cp.start()

    def end():
        for cp in cps:
            cp.wait()

    return start, None, end


def _carry_pairx(units):
    n = len(units)
    return _Carry(units, _pairx_shapes(units), {},
                  [pltpu.SemaphoreType.DMA((n, 4)), pltpu.SemaphoreType.DMA((n, 4))],
                  lambda ins, outs, sems: _pairx_phases(ins, outs, *sems), "sib")


def _rs_pair_add(units, recvs, c):
    n = len(units)

    def body(c_ref, *refs):
        for u in range(n):
            refs[2 * n + u][...] = (refs[u][0].astype(F32) + refs[n + u][...].astype(F32)).astype(BF16)

    in_specs, out_specs, out_shape = [], [], []
    for g in units:
        h, w = g.shape[2] // 2, g.shape[3]
        in_specs.append(pl.BlockSpec((1, 1, h, w), lambda j, i, cr: (j, cr[0], i, 0)))
    for g in units:
        h, w = g.shape[2] // 2, g.shape[3]
        in_specs.append(pl.BlockSpec((1, h, w), lambda j, i, cr: (j, i, 0)))
        out_specs.append(pl.BlockSpec((1, h, w), lambda j, i, cr: (j, i, 0)))
        out_shape.append(jax.ShapeDtypeStruct((4, 2 * h, w), BF16))
    return pl.pallas_call(
        body, name="rs_pair_add",
        grid_spec=pltpu.PrefetchScalarGridSpec(num_scalar_prefetch=1, grid=(4, 2), in_specs=in_specs,
                                               out_specs=out_specs),
        out_shape=_out_hbm(out_shape),
        compiler_params=pltpu.CompilerParams(vmem_limit_bytes=48 << 20))(
            c.reshape(1), *_in_hbm(list(units) + list(recvs)))


def _chipx_phases(p_refs, r_refs, send_sems, recv_sems, rows=None):
    n = len(p_refs)
    x, y, c = _place()
    k = 2 * x + y

    def part(ref):
        return ref if rows is None else ref.at[pl.ds(rows[0], rows[1])]

    sends = []
    for d in range(1, 4):
        j = (k + d) % 4
        for u in range(n):
            sends.append(pltpu.make_async_remote_copy(
                src_ref=part(p_refs[u].at[j]), dst_ref=part(r_refs[u].at[k]), send_sem=send_sems.at[u, d - 1],
                recv_sem=recv_sems.at[u, d - 1], device_id=(j // 2, j % 2, c), device_id_type=MESH))

    def start():
        for cp in sends:
            cp.start()

    def end():
        for d in range(1, 4):
            src = (k + 4 - d) % 4
            for u in range(n):
                pltpu.make_async_remote_copy(
                    src_ref=part(p_refs[u].at[src]), dst_ref=part(r_refs[u].at[src]),
                    send_sem=send_sems.at[u, d - 1], recv_sem=recv_sems.at[u, d - 1], device_id=(x, y, c),
                    device_id_type=MESH).wait_recv()
        for cp in sends:
            cp.wait_send()

    return start, None, end


def _carry_chipx(pairs, rows=None, into=None):
    n = len(pairs)
    sems = [pltpu.SemaphoreType.DMA((n, 3)), pltpu.SemaphoreType.DMA((n, 3))]
    shapes = [jax.ShapeDtypeStruct(p.shape, p.dtype) for p in pairs]
    if into is None:
        return _Carry(pairs, shapes, {}, sems, lambda ins, outs, s: _chipx_phases(ins, outs, *s, rows=rows),
                      "chips")
    return _Carry(list(pairs) + list(into), shapes, {n + u: u for u in range(n)}, sems,
                  lambda ins, outs, s: _chipx_phases(ins[:n], outs, *s, rows=rows), "chips")


def _rs_chip_add(pairs, contribs, c, chip):
    n = len(pairs)

    def body(s_ref, *refs):
        for u in range(n):
            a, b, c_, d = refs[4 * u:4 * u + 4]
            refs[4 * n + u][0] = ((a[0].astype(F32) + b[0].astype(F32)) + c_[0].astype(F32)) + d[0].astype(F32)

    in_specs, out_specs, out_shape, args = [], [], [], []
    for p, r in zip(pairs, contribs):
        h, w = p.shape[1] // 2, p.shape[2]
        in_specs += [pl.BlockSpec((1, h, w), functools.partial(lambda d, i, s: ((s[1] + d) % 4, i, 0), d))
                     for d in range(4)]
        args += [p, r, r, r]
        out_specs.append(pl.BlockSpec((1, h, w), lambda i, s: (s[0], i, 0)))
        out_shape.append(jax.ShapeDtypeStruct((2, 2 * h, w), F32))
    return pl.pallas_call(
        body, name="rs_chip_add",
        grid_spec=pltpu.PrefetchScalarGridSpec(num_scalar_prefetch=1, grid=(2,), in_specs=in_specs,
                                               out_specs=out_specs),
        out_shape=_out_hbm(out_shape),
        compiler_params=pltpu.CompilerParams(vmem_limit_bytes=48 << 20))(jnp.stack([c, chip]), *_in_hbm(args))


def _rs_sibling_gather(reds):
    n = len(reds)

    def body(*refs):
        _peer_barrier("sib")
        start, _, end = _sibx_phases(refs[n:2 * n], *refs[2 * n:])
        start()
        end()

    return pl.pallas_call(
        body, name="rs_sibling_gather", out_shape=[jax.ShapeDtypeStruct(r.shape, r.dtype) for r in reds],
        in_specs=[ANY] * n, out_specs=[ANY] * n, input_output_aliases={u: u for u in range(n)},
        scratch_shapes=[pltpu.SemaphoreType.DMA((n,))] * 2,
        compiler_params=pltpu.CompilerParams(collective_id=BARRIER_IDS["sib"]))(*reds)


def _sibx_phases(o_refs, send_sems, recv_sems):
    n = len(o_refs)
    x, y, c = _place()
    cps = [pltpu.make_async_remote_copy(
        src_ref=o_refs[u].at[c], dst_ref=o_refs[u].at[c], send_sem=send_sems.at[u], recv_sem=recv_sems.at[u],
        device_id=(x, y, 1 - c), device_id_type=MESH) for u in range(n)]

    def start():
        for cp in cps:
            cp.start()

    def end():
        for u in range(n):
            cps[u].wait_send()
            pltpu.make_async_remote_copy(
                src_ref=o_refs[u].at[1 - c], dst_ref=o_refs[u].at[1 - c], send_sem=send_sems.at[u],
                recv_sem=recv_sems.at[u], device_id=(x, y, 1 - c), device_id_type=MESH).wait_recv()

    return start, None, end


def _carry_sibx(reds):
    n = len(reds)
    return _Carry(reds, [jax.ShapeDtypeStruct(r.shape, r.dtype) for r in reds], {u: u for u in range(n)},
                  [pltpu.SemaphoreType.DMA((n,))] * 2, lambda ins, outs, sems: _sibx_phases(outs, *sems), "sib")


def _prologue(blk, c_ctx, w, b, in8):
    n = w.shape[1]

    def body(blk_ref, cctx_ref, w_ref, b_ref, _in_in, g0_ref, c16_ref, g1_ref, in_ref, mod_s,
             s1, r1, l1, s2, r2, l2, s3, r3):
        _peer_barrier("both")
        start, mid, end = _gather_phases([in_ref], s3, r3)
        _gather_blocks([blk_ref], [g0_ref], s1, r1, l1)
        start()
        c16 = jnp.concatenate([g0_ref[i, 0:1, :] for i in range(8)] + [cctx_ref[...], jnp.zeros((7, D), F32)],
                              axis=0)
        c16_ref[...] = c16
        mod_s[...] = _dot(c16 * _sig(c16), w_ref[...], prec=HI) + b_ref[...]
        _gather_blocks([mod_s], [g1_ref], s2, r2, l2)
        mid()
        end()

    vm = pl.BlockSpec(memory_space=pltpu.VMEM)
    return pl.pallas_call(
        body, name="prologue",
        out_shape=[jax.ShapeDtypeStruct((8, 8, D), F32), jax.ShapeDtypeStruct((16, D), F32),
                   jax.ShapeDtypeStruct((8, 16, n), F32), jax.ShapeDtypeStruct(in8.shape, in8.dtype)],
        in_specs=[vm, vm, vm, vm, ANY], out_specs=[vm, vm, vm, ANY], input_output_aliases={4: 3},
        scratch_shapes=[pltpu.VMEM((16, n), F32)] + _gather_sems(1) + _gather_sems(1)
        + [pltpu.SemaphoreType.DMA((1, 7)), pltpu.SemaphoreType.DMA((1, 7))],
        compiler_params=pltpu.CompilerParams(vmem_limit_bytes=48 << 20,
                                             collective_id=BARRIER_IDS["both"]))(blk, c_ctx, w, b, in8)


def _ada_bwd(c16, dmod16, w, carry=None):
    n = w.shape[1]
    tn = 512

    def body(c_ref, d_ref, w_ref, gw_ref, gc_ref):
        j = pl.program_id(0)

        @pl.when(j == 0)
        def _():
            gc_ref[...] = jnp.zeros_like(gc_ref)

        cc = c_ref[...]
        dm = d_ref[...]
        gw_ref[...] = _dot(cc * _sig(cc), dm, TN, prec=HI)
        gc_ref[...] += _dot(dm, w_ref[...], NT, prec=HI)

    return _pcall(body, name="ada_bwd", grid=(n // tn,),
                  in_specs=[_full((16, D)), pl.BlockSpec((16, tn), lambda j: (0, j)),
                            pl.BlockSpec((D, tn), lambda j: (0, j))],
                  out_specs=[pl.BlockSpec((D, tn), lambda j: (0, j)), _full((16, D))],
                  out_shape=[jax.ShapeDtypeStruct((D, n), F32),
                             jax.ShapeDtypeStruct((16, D), F32)], carry=carry)(c16, dmod16, w)


def _adam_math(w, g, m, v):
    c1 = 1.0 - ADAM_B1 ** ADAM_STEP
    c2 = 1.0 - ADAM_B2 ** ADAM_STEP
    nm = ADAM_B1 * m + (1.0 - ADAM_B1) * g
    nv = ADAM_B2 * v + (1.0 - ADAM_B2) * (g * g)
    return -ADAM_LR * ((nm / c1) / (jnp.sqrt(nv / c2) + ADAM_EPS) + ADAM_WD * w), nm, nv


def _adamw_small(ws, gs, ms, vs):
    n = len(ws)

    def body(*refs):
        for u in range(n):
            d_, nm, nv = _adam_math(refs[u][...], refs[n + u][...], refs[2 * n + u][...], refs[3 * n + u][...])
            refs[4 * n + u][...] = d_
            refs[5 * n + u][...] = nm
            refs[6 * n + u][...] = nv

    specs = [_full(w.shape) for w in ws]
    shapes = [jax.ShapeDtypeStruct(w.shape, F32) for w in ws]
    out = _pcall(body, name="adamw_small", grid=(1,), in_specs=specs * 4, out_specs=specs * 3,
                 out_shape=shapes * 3)(*ws, *gs, *ms, *vs)
    return out[:n], out[n:2 * n], out[2 * n:]


def _cctx_grad(parts, c_ctx):
    def body(p_ref, c_ref, o_ref):
        acc = p_ref[0:1, :]
        for k in range(1, 4):
            acc = acc + p_ref[k:k + 1, :]
        cc = c_ref[...]
        s = _sig(cc)
        o_ref[...] = acc * (s * (1.0 + cc * (1.0 - s)))

    return _pcall(body, name="cctx_grad", grid=(1,), in_specs=[_full(parts.shape), _full((1, D))],
                  out_specs=_full((1, D)), out_shape=jax.ShapeDtypeStruct((1, D), F32))(parts, c_ctx)


ADAM_STEPS = 8


def _adamw_multi(ws, gs, ms, vs, *, name, carry=None):
    n = len(ws)

    def body(*refs):
        for u in range(n):
            refs[4 * n + u][...], refs[5 * n + u][...], refs[6 * n + u][...] = _adam_math(
                refs[u][...], refs[n + u][...], refs[2 * n + u][...], refs[3 * n + u][...])

    specs = [pl.BlockSpec((w.shape[0] // ADAM_STEPS, w.shape[1]), lambda i: (i, 0)) for w in ws]
    shapes = [jax.ShapeDtypeStruct(w.shape, F32) for w in ws]
    res = _pcall(body, name=name, grid=(ADAM_STEPS,), in_specs=specs * 4, out_specs=specs * 3,
                 out_shape=shapes * 3, carry=carry)(*ws, *gs, *ms, *vs)
    out, extra = res if carry is not None else (res, None)
    return (out[:n], out[n:2 * n], out[2 * n:]), extra


def kernel(x, c, ctx, c_ctx, w_ada, b_ada, norm_mix_w, norm_ffn_w, w_in, hgrn_lb_logits, hgrn_norm_w, q_norm_w, k_norm_w, attn_sinks, w_branch_hgrn, w_branch_attn, w_out, w_ffn_gate, w_ffn_up, w_ffn_down, loss_target, m_c_ctx, m_w_ada, m_b_ada, m_norm_mix_w, m_norm_ffn_w, m_w_in, m_hgrn_lb_logits, m_hgrn_norm_w, m_q_norm_w, m_k_norm_w, m_attn_sinks, m_w_branch_hgrn, m_w_branch_attn, m_w_out, m_w_ffn_gate, m_w_ffn_up, m_w_ffn_down, v_c_ctx, v_w_ada, v_b_ada, v_norm_mix_w, v_norm_ffn_w, v_w_in, v_hgrn_lb_logits, v_hgrn_norm_w, v_q_norm_w, v_k_norm_w, v_attn_sinks, v_w_branch_hgrn, v_w_branch_attn, v_w_out, v_w_ffn_gate, v_w_ffn_up, v_w_ffn_down):
    xi, yi, ci = _place()
    chip = 2 * xi + yi
    dev = 2 * chip + ci
    s_len = x.shape[1]

    shards = [w_in[0].T, w_branch_hgrn[0], w_branch_attn[0], w_out[0], w_ffn_gate[0].T, w_ffn_up[0].T,
              w_ffn_down[0]]
    bufs = _cast_place(shards, ci, dev)

    lbrow = jnp.pad(hgrn_lb_logits.reshape(1, 512), ((0, 0), (0, D - 512)))
    blk = jnp.concatenate([c, lbrow, jnp.zeros((6, D), F32)], axis=0)
    nada = w_ada.shape[2]
    b_sh = lax.dynamic_slice(b_ada, (0, chip * nada), (1, nada))
    g0, c16, g1, in8 = _prologue(blk, c_ctx[None], w_ada[0], b_sh, bufs[0])
    lg = g0[0::2, 1, :512].reshape(4, 2, 2, 128).transpose(1, 2, 0, 3).reshape(2, 2, HGW)
    modall = g1[0::2].transpose(1, 0, 2).reshape(16, 4 * nada)
    mod = lax.dynamic_slice(modall, (dev, 0), (1, 6 * D)).reshape(6, D)
    modc = modall[8].reshape(6, D)[:2]

    sq, gx, _, small, rs = _local_step(
        x[0], ctx[0], loss_target[0], mod, modc, norm_mix_w, norm_ffn_w, lg, hgrn_norm_w, q_norm_w,
        k_norm_w, attn_sinks[0], in8.reshape(NCOL, D), bufs[1:], dist=(ci, chip))

    def whole(r):
        return r.reshape(2 * r.shape[1], r.shape[2])

    g_dn, g_g, g_u = [whole(r) for r in rs["ffn_done"]]
    g_bh, g_ba, g_o = [whole(r) for r in rs["mix_done"]]
    in_pairs = rs["in_pairs"]
    rest_rows = (in_pairs[0].shape[1] // 2, in_pairs[0].shape[1] // 2)

    g2, tot = _ag_small(small["raw"], sq)
    loss = 0.5 * jnp.sum(tot[24]) / D
    dmodc_tot = jnp.pad(tot[6:8].reshape(1, 2 * D), ((0, 0), (0, 4 * D)))
    g_b_ada = tot[0:6].reshape(1, 6 * D) + dmodc_tot
    dmod16 = jnp.concatenate([g2[:, 0:6].reshape(8, 6 * D), dmodc_tot, jnp.zeros((7, 6 * D), F32)], axis=0)
    (g_w_ada, gc_part), in_contribs = _ada_bwd(
        c16, lax.dynamic_slice(dmod16, (0, chip * nada), (16, nada)), w_ada[0],
        carry=_carry_chipx(in_pairs, rows=rest_rows, into=rs["in_part"]))
    g3, = _allgather([gc_part[8:16]], name="ag_cctx")
    g_c_ctx = _cctx_grad(g3[0::2, 0], c_ctx[None])[0]
    g_nw1 = tot[8:9]
    g_nw2 = tot[9:10]
    g_hw = tot[10, :HGW].reshape(4, HGD).sum(0, keepdims=True)
    g_qnw = tot[10, HGW:].reshape(8, HDIM).sum(0, keepdims=True)
    g_knw = tot[11, :128].reshape(2, HDIM).sum(0, keepdims=True)
    g_sinks = tot[16:24, 0][None]
    g_lg = lax.dynamic_slice(tot[12:16, :HGW].reshape(2, 2, HGW), (0, 0, chip * 128), (2, 2, 128))

    names = ["c_ctx", "w_ada", "b_ada", "norm_mix_w", "norm_ffn_w", "w_in", "hgrn_lb_logits", "hgrn_norm_w",
             "q_norm_w", "k_norm_w", "attn_sinks", "w_branch_hgrn", "w_branch_attn", "w_out", "w_ffn_gate",
             "w_ffn_up", "w_ffn_down"]
    ws = dict(zip(names, [c_ctx, w_ada, b_ada, norm_mix_w, norm_ffn_w, w_in, hgrn_lb_logits, hgrn_norm_w,
                          q_norm_w, k_norm_w, attn_sinks, w_branch_hgrn, w_branch_attn, w_out, w_ffn_gate,
                          w_ffn_up, w_ffn_down]))
    ms = dict(zip(names, [m_c_ctx, m_w_ada, m_b_ada, m_norm_mix_w, m_norm_ffn_w, m_w_in, m_hgrn_lb_logits,
                          m_hgrn_norm_w, m_q_norm_w, m_k_norm_w, m_attn_sinks, m_w_branch_hgrn,
                          m_w_branch_attn, m_w_out, m_w_ffn_gate, m_w_ffn_up, m_w_ffn_down]))
    vs = dict(zip(names, [v_c_ctx, v_w_ada, v_b_ada, v_norm_mix_w, v_norm_ffn_w, v_w_in, v_hgrn_lb_logits,
                          v_hgrn_norm_w, v_q_norm_w, v_k_norm_w, v_attn_sinks, v_w_branch_hgrn,
                          v_w_branch_attn, v_w_out, v_w_ffn_gate, v_w_ffn_up, v_w_ffn_down]))
    transposed = ("w_in", "w_ffn_gate", "w_ffn_up")

    def view(a, n):
        return a[0].T if n in transposed else a[0]

    def unview(a, n):
        return a.T[None] if n in transposed else a[None]

    delta, new_m, new_v, grads = {}, {}, {}, {}

    def big_adamw(group, gs, name, carry=None):
        (d_, m_, v_), extra = _adamw_multi([view(ws[n], n) for n in group], gs, [view(ms[n], n) for n in group],
                                           [view(vs[n], n) for n in group], name=name, carry=carry)
        for i, n in enumerate(group):
            grads[n], delta[n], new_m[n], new_v[n] = (unview(gs[i], n), unview(d_[i], n), unview(m_[i], n),
                                                      unview(v_[i], n))
        return extra

    big_adamw(["w_ffn_down", "w_ffn_gate", "w_ffn_up", "w_out", "w_branch_hgrn", "w_branch_attn"],
              [g_dn, g_g, g_u, g_o, g_bh, g_ba], "adamw_first")
    in_reds = _rs_chip_add(in_pairs, in_contribs, ci, chip)
    g_in, = [whole(r) for r in _rs_sibling_gather(in_reds)]
    big_adamw(["w_in", "w_ada"], [g_in, g_w_ada], "adamw_second")
    grads.update(c_ctx=g_c_ctx, b_ada=g_b_ada, norm_mix_w=g_nw1, norm_ffn_w=g_nw2, hgrn_lb_logits=g_lg,
                 hgrn_norm_w=g_hw, q_norm_w=g_qnw, k_norm_w=g_knw, attn_sinks=g_sinks)
    small_names = [n for n in names if n not in delta]

    def two_d(a):
        return a.reshape(1, -1) if a.ndim == 1 else a

    sd, sm_, sv = _adamw_small(*[[two_d(d[n]) for n in small_names] for d in (ws, grads, ms, vs)])
    for i, n in enumerate(small_names):
        for dst, src in ((delta, sd), (new_m, sm_), (new_v, sv)):
            dst[n] = src[i].reshape(ws[n].shape)
    return (loss, gx[None], *[grads[n] for n in names], *[delta[n] for n in names],
            *[new_m[n] for n in names], *[new_v[n] for n in names])
```

```python
import functools

import numpy as np
import jax
import jax.numpy as jnp
from jax import lax
from jax.experimental import pallas as pl
from jax.experimental.pallas import tpu as pltpu

F32 = jnp.float32
BF16 = jnp.bfloat16
HI = lax.Precision.HIGHEST
MESH = pl.DeviceIdType.MESH

D = 1024
L = 256
TM = 256
HGW = 512
HGD = 128
CH = 32
ATW = 512
HDIM = 64
BLK = 128
GRID_W = 64
DFF = 2816
NCOL = 5376
EPS = 1e-6
ROPE_THETA = 10000.0
BF16_SUBLANES = 16

C_FB, C_INP, C_QHG, C_FF = 0, 1, 2, 3
C_GATES = 1
C_GHG, C_QRAW = 8, 9
C_KV = 20
C_QKV = 6

ADAM_LR, ADAM_B1, ADAM_B2, ADAM_EPS, ADAM_WD, ADAM_STEP = 0.001, 0.9, 0.999, 1e-08, 0.01, 10

NN = (((1,), (0,)), ((), ()))
NT = (((1,), (1,)), ((), ()))
TN = (((0,), (0,)), ((), ()))


def _dot(a, b, dims=NN, prec=None):
    return lax.dot_general(a, b, dims, precision=prec, preferred_element_type=F32)


def _bdot(a, b, dims=NN):
    return _dot(a.astype(BF16), b.astype(BF16), dims)


def _sig(x):
    return 1.0 / (1.0 + jnp.exp(-x))


class _Carry:
    def __init__(self, ins, outs, aliases, scratch, phases, peers):
        self.ins, self.outs, self.aliases, self.scratch, self.phases = ins, outs, aliases, scratch, phases
        self.peers = peers


BARRIER_IDS = {"sib": 1, "chips": 2, "both": 3}


def _peer_barrier(kind):
    x, y, c = _place()
    peers = []
    if kind in ("sib", "both"):
        peers.append((x, y, 1 - c))
    if kind in ("chips", "both"):
        peers += [(1 - x, y, c), (x, 1 - y, c), (1 - x, 1 - y, c)]
    bar = pltpu.get_barrier_semaphore()
    for peer in peers:
        pl.semaphore_signal(bar, inc=1, device_id=peer, device_id_type=MESH)
    pl.semaphore_wait(bar, len(peers))


def _in_hbm(args):
    return [pltpu.with_memory_space_constraint(a, pltpu.HBM) for a in args]


def _out_hbm(shapes):
    if isinstance(shapes, (list, tuple)):
        return [pltpu.HBM(s.shape, s.dtype) for s in shapes]
    return pltpu.HBM(shapes.shape, shapes.dtype)


def _carry_join(a, b):
    na_in, na_out, na_sc = len(a.ins), len(a.outs), len(a.scratch)
    aliases = dict(a.aliases)
    aliases.update({na_in + i: na_out + o for i, o in b.aliases.items()})

    def phases(ins, outs, sems):
        pa = a.phases(ins[:na_in], outs[:na_out], sems[:na_sc])
        pb = b.phases(ins[na_in:], outs[na_out:], sems[na_sc:])

        def both(fa, fb):
            if fa is None and fb is None:
                return None

            def run():
                for fn in (fa, fb):
                    if fn is not None:
                        fn()
            return run

        return tuple(both(fa, fb) for fa, fb in zip(pa, pb))

    return _Carry(list(a.ins) + list(b.ins), list(a.outs) + list(b.outs), aliases,
                  list(a.scratch) + list(b.scratch), phases, a.peers if a.peers == b.peers else "both")


def _pcall(body, *, name, grid, in_specs, out_specs, out_shape, scratch=(), aliases=None, vmem_mb=48,
           carry=None):
    params = pltpu.CompilerParams(dimension_semantics=("arbitrary",) * len(grid),
                                  vmem_limit_bytes=vmem_mb << 20)
    if carry is None:
        plain = pl.pallas_call(
            body, name=name, grid=grid, in_specs=in_specs, out_specs=out_specs, out_shape=_out_hbm(out_shape),
            scratch_shapes=list(scratch), input_output_aliases=aliases or {}, compiler_params=params)
        return lambda *args: plain(*_in_hbm(args))
    single = not isinstance(out_shape, (list, tuple))
    out_specs_l = [out_specs] if single else list(out_specs)
    out_shape_l = [out_shape] if single else list(out_shape)
    n_in, n_out, n_sc = len(in_specs), len(out_shape_l), len(scratch)
    k_in, k_out = len(carry.ins), len(carry.outs)
    nsteps = int(np.prod(grid))
    assert nsteps >= 3

    def wrapped(*refs):
        ins, cins = refs[:n_in], refs[n_in:n_in + k_in]
        o0 = n_in + k_in
        outs, couts = refs[o0:o0 + n_out], refs[o0 + n_out:o0 + n_out + k_out]
        s0 = o0 + n_out + k_out
        sc, csc = refs[s0:s0 + n_sc], refs[s0 + n_sc:]
        step = pl.program_id(0)
        for ax in range(1, len(grid)):
            step = step * grid[ax] + pl.program_id(ax)
        start, mid, end = carry.phases(cins, couts, csc)

        @pl.when(step == 0)
        def _():
            _peer_barrier(carry.peers)
            start()

        body(*ins, *outs, *sc)
        if mid is not None:
            pl.when(step == nsteps - 2)(mid)
        pl.when(step == nsteps - 1)(end)

    all_aliases = dict(aliases or {})
    all_aliases.update({n_in + i: n_out + o for i, o in carry.aliases.items()})
    call = pl.pallas_call(
        wrapped, name=name, grid=grid, in_specs=list(in_specs) + [ANY] * k_in,
        out_specs=out_specs_l + [ANY] * k_out, out_shape=_out_hbm(out_shape_l + list(carry.outs)),
        scratch_shapes=list(scratch) + list(carry.scratch), input_output_aliases=all_aliases,
        compiler_params=pltpu.CompilerParams(dimension_semantics=("arbitrary",) * len(grid),
                                             vmem_limit_bytes=vmem_mb << 20,
                                             collective_id=BARRIER_IDS[carry.peers]))

    def run(*args):
        res = call(*_in_hbm(args), *carry.ins)
        core = res[:n_out]
        return (core[0] if single else list(core)), list(res[n_out:])

    return run


def _full(shape):
    nd = len(shape)
    return pl.BlockSpec(shape, lambda *_: (0,) * nd)


ANY = pl.BlockSpec(memory_space=pl.ANY)


NT_IN = NCOL // 256


def _src_block(j):
    return j + jnp.where(j < 4, 2, jnp.where(j < 6, 3, jnp.where(j < 8, -6, jnp.where(
        j < 16, 5, jnp.where(j < 20, -7, -14)))))


def _mm_in(h, wt, tm, carry=None):
    tt = h.shape[0]

    def body(h_ref, w_ref, o_ref):
        o_ref[...] = _bdot(h_ref[...], w_ref[...], NT)

    return _pcall(body, name="mm_in", grid=(tt // tm, NT_IN),
                  in_specs=[pl.BlockSpec((tm, D), lambda i, j: (i, 0)),
                            pl.BlockSpec((256, D), lambda i, j: (_src_block(j), 0))],
                  out_specs=pl.BlockSpec((tm, 256), lambda i, j: (i, j)),
                  out_shape=jax.ShapeDtypeStruct((tt, NCOL), F32), carry=carry)(h, wt)


def _mm_dh(dp, wt, tm, carry=None):
    tt = dp.shape[0]
    per, ng = 3, NT_IN // 3

    def body(d_ref, w0, w1, w2, o_ref, acc):
        kk = pl.program_id(1)

        @pl.when(kk == 0)
        def _():
            acc[...] = jnp.zeros_like(acc)

        acc[...] += (_bdot(d_ref[:, 0:256], w0[...]) + _bdot(d_ref[:, 256:512], w1[...])
                     + _bdot(d_ref[:, 512:768], w2[...]))

        @pl.when(kk == ng - 1)
        def _():
            o_ref[...] = acc[...]

    wspecs = [pl.BlockSpec((256, D), functools.partial(lambda t, i, kk: (_src_block(per * kk + t), 0), t))
              for t in range(per)]
    return _pcall(body, name="mm_dh", grid=(tt // tm, ng),
                  in_specs=[pl.BlockSpec((tm, per * 256), lambda i, kk: (i, kk))] + wspecs,
                  out_specs=pl.BlockSpec((tm, D), lambda i, kk: (i, 0)),
                  out_shape=jax.ShapeDtypeStruct((tt, D), F32), scratch=[pltpu.VMEM((tm, D), F32)],
                  carry=carry)(dp, wt, wt, wt)


def _mm_gin(dp, h, tk):
    tt = dp.shape[0]
    nk = tt // tk

    def body(d_ref, h_ref, o_ref, acc):
        kk = pl.program_id(1)

        @pl.when(kk == 0)
        def _():
            acc[...] = jnp.zeros_like(acc)

        acc[...] += _bdot(d_ref[...], h_ref[...], TN)

        @pl.when(kk == nk - 1)
        def _():
            o_ref[...] = acc[...].astype(BF16)

    return _pcall(body, name="mm_gin", grid=(NT_IN, nk),
                  in_specs=[pl.BlockSpec((tk, 256), lambda j, kk: (kk, j)),
                            pl.BlockSpec((tk, D), lambda j, kk: (kk, 0))],
                  out_specs=pl.BlockSpec((256, D), lambda j, kk: (_src_block(j), 0)),
                  out_shape=jax.ShapeDtypeStruct((NCOL, D), BF16), scratch=[pltpu.VMEM((256, D), F32)])(dp, h)


def _tok_specs():
    assert L == TM
    return [_full((TM, D)), pl.BlockSpec((TM, D), lambda i: (jnp.maximum(i - 1, 0), 0))]


def _mod1(ctx, x, nw, ss):
    rows = L + x.shape[0]

    def body(c_ref, x_ref, nw_ref, ss_ref, h_ref):
        t = jnp.where(pl.program_id(0) == 0, c_ref[...], x_ref[...])
        r = lax.rsqrt(jnp.mean(t * t, axis=-1, keepdims=True) + EPS)
        s = ss_ref[0]
        h_ref[...] = ((t * r * nw_ref[...]) * (1.0 + s[1:2]) + s[0:1]).astype(BF16)

    return _pcall(body, name="mod1", grid=(rows // TM,),
                  in_specs=_tok_specs() + [_full((1, D)),
                                           pl.BlockSpec((1, 2, D), lambda i: (jnp.minimum(i, 1), 0, 0))],
                  out_specs=pl.BlockSpec((TM, D), lambda i: (i, 0)),
                  out_shape=jax.ShapeDtypeStruct((rows, D), BF16))(ctx, x, nw, ss)


def _norm_bwd_rows(x, dh, nw, scale):
    r = lax.rsqrt(jnp.mean(x * x, axis=-1, keepdims=True) + EPS)
    xh = x * r
    dxh = dh * ((1.0 + scale) * nw)
    dx = r * (dxh - xh * jnp.mean(dxh * xh, axis=-1, keepdims=True))
    return dx, xh


def _out_proj_mod2(mixed, w_o, x, g1, nw2, ss2):
    s_len = x.shape[0]
    tm = 512

    def body(m_ref, w_ref, x_ref, g_ref, nw_ref, ss_ref, ao_ref, x1_ref, h_ref):
        ao = _bdot(m_ref[...], w_ref[...])
        ao_ref[...] = ao.astype(BF16)
        x1 = x_ref[...] + g_ref[...] * ao
        x1_ref[...] = x1
        r = lax.rsqrt(jnp.mean(x1 * x1, axis=-1, keepdims=True) + EPS)
        s = ss_ref[0]
        h_ref[...] = ((x1 * r * nw_ref[...]) * (1.0 + s[1:2]) + s[0:1]).astype(BF16)

    row = pl.BlockSpec((tm, D), lambda i: (i, 0))
    f = jax.ShapeDtypeStruct((s_len, D), F32)
    return _pcall(body, name="out_proj_mod2", grid=(s_len // tm,),
                  in_specs=[row, _full((D, D)), row, _full((1, D)), _full((1, D)), _full((1, 2, D))],
                  out_specs=[row, row, row],
                  out_shape=[jax.ShapeDtypeStruct((s_len, D), BF16), f,
                             jax.ShapeDtypeStruct((s_len, D), BF16)])(mixed, w_o, x, g1, nw2, ss2)


TS = 1024


def _acc_call(body, *, name, grid, in_specs, out_specs, out_shape, acc_shapes, args, carry=None):
    return _pcall(body, name=name, grid=grid, in_specs=in_specs, out_specs=out_specs, out_shape=out_shape,
                  scratch=[pltpu.VMEM(s, F32) for s in acc_shapes], carry=carry)(*args)


def _ffn_up(h2, g4, u4, carry=None):
    s_len = h2.shape[0]
    ns = g4.shape[1]

    def body(h_ref, g_ref, u_ref, a_ref, b_ref, z_ref):
        h = h_ref[...]
        a = _bdot(h, g_ref[0], NT)
        b = _bdot(h, u_ref[0], NT)
        a_ref[0] = a.astype(BF16)
        b_ref[0] = b.astype(BF16)
        z_ref[0] = (a * _sig(a) * b).astype(BF16)

    w = pl.BlockSpec((1, ns, D), lambda i, j: (j, 0, 0))
    o = pl.BlockSpec((1, TS, ns), lambda i, j: (j, i, 0))
    f = jax.ShapeDtypeStruct((4, s_len, ns), BF16)
    return _pcall(body, name="ffn_up", grid=(s_len // TS, 4),
                  in_specs=[pl.BlockSpec((TS, D), lambda i, j: (i, 0)), w, w], out_specs=[o, o, o],
                  out_shape=[f, f, jax.ShapeDtypeStruct((4, s_len, ns), BF16)], carry=carry)(h2, g4, u4)


def _ffn_down_loss(z4, dn4, x1, g2, tgt):
    _, s_len, ns = z4.shape

    def body(z_ref, w_ref, x1_ref, g_ref, t_ref, sq_ref, dx2_ref, dyb_ref, dg_ref, acc):
        i, j = pl.program_id(0), pl.program_id(1)

        @pl.when((i == 0) & (j == 0))
        def _():
            sq_ref[...] = jnp.zeros_like(sq_ref)
            dg_ref[...] = jnp.zeros_like(dg_ref)

        @pl.when(j == 0)
        def _():
            acc[...] = jnp.zeros_like(acc)

        acc[...] += _bdot(z_ref[0], w_ref[0])

        @pl.when(j == 3)
        def _():
            y_ = acc[...]
            g = g_ref[...]
            e = x1_ref[...] + g * y_ - t_ref[...]
            sq_ref[...] += jnp.sum(e * e, axis=0, keepdims=True)
            dx2 = e * (1.0 / D)
            dx2_ref[...] = dx2
            dyb_ref[...] = (g * dx2).astype(BF16)
            dg_ref[...] += jnp.sum(dx2 * y_, axis=0, keepdims=True)

    row = pl.BlockSpec((TS, D), lambda i, j: (i, 0))
    vec = _full((1, D))
    return _acc_call(body, name="ffn_down_loss", grid=(s_len // TS, 4),
                     in_specs=[pl.BlockSpec((1, TS, ns), lambda i, j: (j, i, 0)),
                               pl.BlockSpec((1, ns, D), lambda i, j: (j, 0, 0)), row, vec, row],
                     out_specs=[vec, row, row, vec],
                     out_shape=[jax.ShapeDtypeStruct((1, D), F32), jax.ShapeDtypeStruct((s_len, D), F32),
                                jax.ShapeDtypeStruct((s_len, D), BF16), jax.ShapeDtypeStruct((1, D), F32)],
                     acc_shapes=[(TS, D)], args=(z4, dn4, x1, g2, tgt))


def _ffn_dz(dyb, dn4, a4, b4):
    _, s_len, ns = a4.shape

    def body(dy_ref, w_ref, a_ref, b_ref, da_ref, db_ref):
        dz = _bdot(dy_ref[...], w_ref[0], NT)
        a = a_ref[0].astype(F32)
        s = _sig(a)
        da_ref[0] = (dz * b_ref[0].astype(F32) * (s * (1.0 + a * (1.0 - s)))).astype(BF16)
        db_ref[0] = (dz * (a * s)).astype(BF16)

    t = pl.BlockSpec((1, TS, ns), lambda i, j: (j, i, 0))
    o = jax.ShapeDtypeStruct((4, s_len, ns), BF16)
    return _pcall(body, name="ffn_dz", grid=(s_len // TS, 4),
                  in_specs=[pl.BlockSpec((TS, D), lambda i, j: (i, 0)),
                            pl.BlockSpec((1, ns, D), lambda i, j: (j, 0, 0)), t, t],
                  out_specs=[t, t], out_shape=[o, o])(dyb, dn4, a4, b4)


def _ffn_gdn(z4, dyb):
    _, s_len, ns = z4.shape
    tk = min(s_len, 2 * TS)
    nk = s_len // tk

    def body(z_ref, dy_ref, o_ref, acc):
        t = pl.program_id(1)

        @pl.when(t == 0)
        def _():
            acc[...] = jnp.zeros_like(acc)

        acc[...] += _bdot(z_ref[0], dy_ref[...], TN)

        @pl.when(t == nk - 1)
        def _():
            o_ref[0] = acc[...].astype(o_ref.dtype)

    return _acc_call(body, name="ffn_gdn", grid=(4, nk),
                     in_specs=[pl.BlockSpec((1, tk, ns), lambda j, t: (j, t, 0)),
                               pl.BlockSpec((tk, D), lambda j, t: (t, 0))],
                     out_specs=pl.BlockSpec((1, ns, D), lambda j, t: (j, 0, 0)),
                     out_shape=jax.ShapeDtypeStruct((4, ns, D), BF16), acc_shapes=[(ns, D)], args=(z4, dyb))


def _ffn_dh2(da4, db4, g4, u4, carry=None):
    _, s_len, ns = da4.shape

    def body(da_ref, db_ref, g_ref, u_ref, o_ref, acc):
        j = pl.program_id(1)

        @pl.when(j == 0)
        def _():
            acc[...] = jnp.zeros_like(acc)

        acc[...] += _bdot(da_ref[0], g_ref[0]) + _bdot(db_ref[0], u_ref[0])

        @pl.when(j == 3)
        def _():
            o_ref[...] = acc[...]

    t = pl.BlockSpec((1, TS, ns), lambda i, j: (j, i, 0))
    w = pl.BlockSpec((1, ns, D), lambda i, j: (j, 0, 0))
    return _acc_call(body, name="ffn_dh2", grid=(s_len // TS, 4), in_specs=[t, t, w, w],
                     out_specs=pl.BlockSpec((TS, D), lambda i, j: (i, 0)),
                     out_shape=jax.ShapeDtypeStruct((s_len, D), F32), acc_shapes=[(TS, D)],
                     args=(da4, db4, g4, u4), carry=carry)


def _ffn_ggu(h2, da4, db4, carry=None):
    _, s_len, ns = da4.shape
    nk = s_len // TS

    def body(h_ref, da_ref, db_ref, gg_ref, gu_ref, acc_g, acc_u):
        t = pl.program_id(1)

        @pl.when(t == 0)
        def _():
            acc_g[...] = jnp.zeros_like(acc_g)
            acc_u[...] = jnp.zeros_like(acc_u)

        h = h_ref[...]
        acc_g[...] += _bdot(da_ref[0], h, TN)
        acc_u[...] += _bdot(db_ref[0], h, TN)

        @pl.when(t == nk - 1)
        def _():
            gg_ref[0] = acc_g[...].astype(BF16)
            gu_ref[0] = acc_u[...].astype(BF16)

    d = pl.BlockSpec((1, TS, ns), lambda j, t: (j, t, 0))
    o = pl.BlockSpec((1, ns, D), lambda j, t: (j, 0, 0))
    f = jax.ShapeDtypeStruct((4, ns, D), BF16)
    return _acc_call(body, name="ffn_ggu", grid=(4, nk),
                     in_specs=[pl.BlockSpec((TS, D), lambda j, t: (t, 0)), d, d], out_specs=[o, o],
                     out_shape=[f, f], acc_shapes=[(ns, D), (ns, D)], args=(h2, da4, db4), carry=carry)


def _mod2_bwd(x1, dh2, dx2, ao, nw2, ss2, g1):
    s_len = x1.shape[0]

    def body(x1_ref, dh_ref, dx2_ref, ao_ref, nw_ref, ss_ref, g_ref,
             dx1_ref, da_ref, dss_ref, dnw_ref, dg_ref):
        i = pl.program_id(0)

        @pl.when(i == 0)
        def _():
            dss_ref[...] = jnp.zeros_like(dss_ref)
            dnw_ref[...] = jnp.zeros_like(dnw_ref)
            dg_ref[...] = jnp.zeros_like(dg_ref)

        dh = dh_ref[...]
        nw = nw_ref[...]
        scale = ss_ref[0][1:2]
        dxn, xh = _norm_bwd_rows(x1_ref[...], dh, nw, scale)
        dx1 = dx2_ref[...] + dxn
        dx1_ref[...] = dx1
        da_ref[...] = (g_ref[...] * dx1).astype(BF16)
        dg_ref[...] += jnp.sum(dx1 * ao_ref[...].astype(F32), axis=0, keepdims=True)
        dsh = jnp.sum(dh, axis=0, keepdims=True)
        dsc = jnp.sum(dh * xh * nw, axis=0, keepdims=True)
        dss_ref[...] += jnp.concatenate([dsh, dsc], axis=0)
        dnw_ref[...] += jnp.sum(dh * xh * (1.0 + scale), axis=0, keepdims=True)

    row = pl.BlockSpec((TM, D), lambda i: (i, 0))
    vec = _full((1, D))
    return _pcall(body, name="mod2_bwd", grid=(s_len // TM,),
                  in_specs=[row, row, row, row, vec, _full((1, 2, D)), vec],
                  out_specs=[row, row, _full((2, D)), vec, vec],
                  out_shape=[jax.ShapeDtypeStruct((s_len, D), F32), jax.ShapeDtypeStruct((s_len, D), BF16),
                             jax.ShapeDtypeStruct((2, D), F32), jax.ShapeDtypeStruct((1, D), F32),
                             jax.ShapeDtypeStruct((1, D), F32)])(x1, dh2, dx2, ao, nw2, ss2, g1)


def _mod1_bwd(ctx, x, dh, dx1, nw1, ss1):
    s_len = dx1.shape[0]
    tt = L + s_len

    def body(c_ref, x_ref, dh_ref, dx1_ref, nw_ref, ss_ref, dx_ref, dss_ref, dnw_ref):
        i = pl.program_id(0)
        tok = jnp.where(i == 0, c_ref[...], x_ref[...])

        @pl.when(i == 0)
        def _():
            dnw_ref[...] = jnp.zeros_like(dnw_ref)

        @pl.when(i <= 1)
        def _():
            dss_ref[...] = jnp.zeros_like(dss_ref)

        dh_ = dh_ref[...]
        nw = nw_ref[...]
        scale = ss_ref[0][1:2]
        dxn, xh = _norm_bwd_rows(tok, dh_, nw, scale)

        @pl.when(i >= 1)
        def _():
            dx_ref[...] = dx1_ref[...] + dxn

        dsh = jnp.sum(dh_, axis=0, keepdims=True)
        dsc = jnp.sum(dh_ * xh * nw, axis=0, keepdims=True)
        dss_ref[...] += jnp.concatenate([dsh, dsc], axis=0)[None]
        dnw_ref[...] += jnp.sum(dh_ * xh * (1.0 + scale), axis=0, keepdims=True)

    row = pl.BlockSpec((TM, D), lambda i: (i, 0))
    lat = pl.BlockSpec((TM, D), lambda i: (jnp.maximum(i - 1, 0), 0))
    sel = pl.BlockSpec((1, 2, D), lambda i: (jnp.minimum(i, 1), 0, 0))
    return _pcall(body, name="mod1_bwd", grid=(tt // TM,),
                  in_specs=_tok_specs() + [row, lat, _full((1, D)), sel],
                  out_specs=[lat, sel, _full((1, D))],
                  out_shape=[jax.ShapeDtypeStruct((s_len, D), F32), jax.ShapeDtypeStruct((2, 2, D), F32),
                             jax.ShapeDtypeStruct((1, D), F32)])(ctx, x, dh, dx1, nw1, ss1)


def _rows(c):
    return slice(c * CH, (c + 1) * CH)


def _chunk_masks(rev, transpose=False):
    r = lax.broadcasted_iota(jnp.int32, (TM, TM), 0)
    c = lax.broadcasted_iota(jnp.int32, (TM, TM), 1)
    same = (r // CH) == (c // CH)
    before = (c >= r) if (rev != transpose) else (c <= r)
    return same & before, same


def _chunk_scan(x, rev, transpose=False):
    r = lax.broadcasted_iota(jnp.int32, (CH, CH), 0)
    c = lax.broadcasted_iota(jnp.int32, (CH, CH), 1)
    tri = ((c >= r) if (rev != transpose) else (c <= r)).astype(F32)
    return jnp.concatenate([_dot(tri, x[_rows(ch)], prec=HI) for ch in range(x.shape[0] // CH)], axis=0)


def _chunk_total(x):
    return jnp.concatenate([jnp.broadcast_to(jnp.sum(x[_rows(ch)], axis=0, keepdims=True), (CH, x.shape[1]))
                            for ch in range(x.shape[0] // CH)], axis=0)


def _hgrn_gate(fl, qraw, lg):
    lb = 1.0 / (1.0 + jnp.exp(lg[1:2] - lg[0:1]))
    sg = _sig(fl)
    f = lb + (1.0 - lb) * sg
    q = qraw * _sig(qraw) * (HGD ** -0.5)
    return lb, sg, f, q


def _hgrn_fwd(p, lg, *, rev, carry=None, readout=None):
    tt = p.shape[0]
    nt = tt // TM
    ncht = TM // CH
    d = 1 if rev else 0

    def tile_of(s):
        return jnp.where(s == 0, 0, nt - s) if rev else s

    def body(*refs):
        if readout is None:
            f_ref, inp_ref, q_ref, lg_ref, o_ref, st_ref, state = refs
        else:
            f_ref, inp_ref, q_ref, lg_ref, oo_ref, g_ref, hw_ref, o_ref, st_ref, y_ref, state = refs
        s = pl.program_id(0)

        @pl.when(s == 0)
        def _():
            state[...] = jnp.zeros_like(state)

        _, _, f, q = _hgrn_gate(f_ref[...], q_ref[...], lg_ref[0])
        lf = jnp.log(f)
        causal, _ = _chunk_masks(rev)
        cum = _chunk_scan(lf, rev)
        tot = _chunk_total(lf)
        qd = (q * jnp.exp(cum)).astype(BF16)
        kd = ((1.0 - f) * jnp.exp(-cum)).astype(BF16)
        ke = ((1.0 - f) * jnp.exp(tot - cum)).astype(BF16)
        et = jnp.exp(tot)
        v = inp_ref[...].astype(BF16)
        order = range(ncht - 1, -1, -1) if rev else range(ncht)
        outs = []
        for h in range(4):
            sl = slice(h * HGD, (h + 1) * HGD)
            qd_, kd_, ke_, v_ = qd[:, sl], kd[:, sl], ke[:, sl], v[:, sl]
            pm = jnp.where(causal, _dot(qd_, kd_, NT), 0.0).astype(BF16)
            o_h = _dot(pm, v_)
            upd = [_dot(v_[_rows(c)], ke_[_rows(c)], TN) for c in range(ncht)]
            st = state[h]
            for c in order:
                st_ref[c, h] = st
                st = st * et[c * CH:c * CH + 1, sl] + upd[c]
            state[h] = st
            inter = [_dot(qd_[_rows(c)], st_ref[c, h].astype(BF16), NT) for c in range(ncht)]
            outs.append(o_h + jnp.concatenate(inter, axis=0))
        o_tile = jnp.concatenate(outs, axis=1)
        o_ref[...] = o_tile
        if readout is not None:
            @pl.when(tile_of(s) >= 1)
            def _():
                g = g_ref[...]
                y_ref[...] = (_head_rms(oo_ref[...] + o_tile, None, 4) * hw_ref[...] * (g * _sig(g))).astype(BF16)

    def col(cb):
        return pl.BlockSpec((TM, HGW), lambda s: (tile_of(s), cb))

    in_specs = [col(C_FB if rev else C_FF), col(C_INP), col(C_QHG), pl.BlockSpec((1, 2, HGW), lambda s: (d, 0, 0))]
    out_specs = [col(0), pl.BlockSpec((ncht, 4, HGD, HGD), lambda s: (tile_of(s), 0, 0, 0))]
    out_shape = [jax.ShapeDtypeStruct((tt, HGW), F32), jax.ShapeDtypeStruct((nt * ncht, 4, HGD, HGD), F32)]
    args = [p, p, p, lg]
    if readout is not None:
        in_specs += [col(0), col(C_GHG), _full((1, HGW))]
        args += [readout[0], p, readout[1]]
        assert rev
        out_specs.append(pl.BlockSpec((TM, HGW), lambda s: (jnp.where(s == 0, nt - 2, tile_of(s) - 1), 0)))
        out_shape.append(jax.ShapeDtypeStruct((tt - L, HGW), BF16))
    return _pcall(body, name="hgrn_fwd_rev" if rev else "hgrn_fwd", grid=(nt,), in_specs=in_specs,
                  out_specs=out_specs, out_shape=out_shape, scratch=[pltpu.VMEM((4, HGD, HGD), F32)],
                  carry=carry)(*args)


def _hgrn_bwd(p, lg, do, st, dp, prev, *, rev, carry=None):
    tt = p.shape[0]
    nt = tt // TM
    ncht = TM // CH
    d = 1 if rev else 0
    second = prev is not None

    def tile_of(s):
        return jnp.where(s == nt - 1, 0, s + 1) if rev else nt - 1 - s

    def body(*refs):
        if second:
            (f_ref, inp_ref, q_ref, lg_ref, do_ref, st_ref, dvp_ref, dqp_ref, _dp_in,
             dp_ref, dlg_ref, dstate) = refs
        else:
            (f_ref, inp_ref, q_ref, lg_ref, do_ref, st_ref, _dp_in,
             dp_ref, dv_ref, dq_ref, dlg_ref, dstate) = refs
        s = pl.program_id(0)
        tile = tile_of(s)

        @pl.when(s == 0)
        def _():
            dstate[...] = jnp.zeros_like(dstate)
            dlg_ref[...] = jnp.zeros_like(dlg_ref)

        qraw = q_ref[...]
        lb, sg, f, q = _hgrn_gate(f_ref[...], qraw, lg_ref[0])
        lf = jnp.log(f)
        causal, _ = _chunk_masks(rev)
        causal_t, _ = _chunk_masks(rev, transpose=True)
        cum = _chunk_scan(lf, rev)
        tot = _chunk_total(lf)
        ea, eb, ee, et = jnp.exp(cum), jnp.exp(-cum), jnp.exp(tot - cum), jnp.exp(tot)
        qdf, kdf, kef = q * ea, (1.0 - f) * eb, (1.0 - f) * ee
        qd, kd, ke = qdf.astype(BF16), kdf.astype(BF16), kef.astype(BF16)
        v = inp_ref[...].astype(BF16)
        dob = jnp.where(tile == 0, 0.0, do_ref[...]).astype(BF16)
        order = range(ncht) if rev else range(ncht - 1, -1, -1)
        dq_l, dk_l, dv_l, dcum_l, dtot_l = [], [], [], [], []
        for h in range(4):
            sl = slice(h * HGD, (h + 1) * HGD)
            qd_, kd_, ke_, v_, do_ = qd[:, sl], kd[:, sl], ke[:, sl], v[:, sl], dob[:, sl]
            pmt = jnp.where(causal_t, _dot(kd_, qd_, NT), 0.0).astype(BF16)
            dpm = jnp.where(causal, _dot(do_, v_, NT), 0.0).astype(BF16)
            dpmt = jnp.where(causal_t, _dot(v_, do_, NT), 0.0).astype(BF16)
            dv = _dot(pmt, do_)
            dqd = _dot(dpm, kd_)
            dkd = _dot(dpmt, qd_)
            upd = [_dot(do_[_rows(c)], qd_[_rows(c)], TN) for c in range(ncht)]
            ds = dstate[h]
            ds1 = [None] * ncht
            for c in order:
                ds1[c] = ds
                ds = ds * et[c * CH:c * CH + 1, sl] + upd[c]
            dstate[h] = ds
            dke_c, dv_c, dqd_c, dtot_c = [], [], [], []
            for c in range(ncht):
                st0 = st_ref[c, h]
                dsb = ds1[c].astype(BF16)
                dke_ = _dot(v_[_rows(c)], dsb)
                dke_c.append(dke_)
                dv_c.append(_dot(ke_[_rows(c)], dsb, NT))
                dqd_c.append(_dot(do_[_rows(c)], st0.astype(BF16)))
                dt = (jnp.sum(ds1[c] * st0, axis=0, keepdims=True) * et[c * CH:c * CH + 1, sl]
                      + jnp.sum(dke_ * kef[_rows(c), sl], axis=0, keepdims=True))
                dtot_c.append(jnp.broadcast_to(dt, (CH, HGD)))
            dke = jnp.concatenate(dke_c, axis=0)
            dqd = dqd + jnp.concatenate(dqd_c, axis=0)
            dv_l.append(dv + jnp.concatenate(dv_c, axis=0))
            dtot_l.append(jnp.concatenate(dtot_c, axis=0))
            dq_l.append(dqd * ea[:, sl])
            dk_l.append(dkd * eb[:, sl] + dke * ee[:, sl])
            dcum_l.append(dqd * qdf[:, sl] - dkd * kdf[:, sl] - dke * kef[:, sl])
        dcum = jnp.concatenate(dcum_l, axis=1)
        dlf = _chunk_scan(dcum, rev, transpose=True) + jnp.concatenate(dtot_l, axis=1)
        dq_t = jnp.concatenate(dq_l, axis=1)
        dv_t = jnp.concatenate(dv_l, axis=1)

        df = dlf / f - jnp.concatenate(dk_l, axis=1)
        dfl = df * (1.0 - lb) * sg * (1.0 - sg)
        dlb = jnp.sum(df * (1.0 - sg), axis=0, keepdims=True)
        dl0 = dlb * lb * (1.0 - lb)
        dlg_ref[...] += jnp.concatenate([dl0, -dl0], axis=0)[None]
        if second:
            sq = _sig(qraw)
            dqr = (dqp_ref[...] + dq_t) * (HGD ** -0.5) * (sq * (1.0 + qraw * (1.0 - sq)))
            dp_ref[...] = jnp.concatenate([dfl, dvp_ref[...] + dv_t, dqr], axis=1).astype(BF16)
        else:
            dp_ref[...] = dfl.astype(BF16)
            dv_ref[...] = dv_t
            dq_ref[...] = dq_t

    def col(cb):
        return pl.BlockSpec((TM, HGW), lambda s: (tile_of(s), cb))

    tok = pl.BlockSpec((TM, HGW), lambda s: (tile_of(s), 0))
    in_specs = [col(C_FB if rev else C_FF), col(C_INP), col(C_QHG),
                pl.BlockSpec((1, 2, HGW), lambda s: (d, 0, 0)),
                pl.BlockSpec((TM, HGW), lambda s: (jnp.maximum(tile_of(s) - 1, 0), 0)),
                pl.BlockSpec((ncht, 4, HGD, HGD), lambda s: (tile_of(s), 0, 0, 0))]
    args = [p, p, p, lg, do, st]
    dlg_spec = _full((1, 2, HGW))
    dlg_shape = jax.ShapeDtypeStruct((1, 2, HGW), F32)
    if second:
        in_specs += [tok, tok]
        args += [prev[0], prev[1]]
        out_specs = [pl.BlockSpec((TM, 3 * HGW), lambda s: (tile_of(s), 0)), dlg_spec]
        out_shape = [jax.ShapeDtypeStruct(dp.shape, BF16), dlg_shape]
    else:
        out_specs = [pl.BlockSpec((TM, HGW), lambda s: (tile_of(s), C_FB if rev else C_FF)), tok, tok, dlg_spec]
        out_shape = [jax.ShapeDtypeStruct(dp.shape, BF16), jax.ShapeDtypeStruct((tt, HGW), F32),
                     jax.ShapeDtypeStruct((tt, HGW), F32), dlg_shape]
    in_specs.append(ANY)
    args.append(dp)
    return _pcall(body, name="hgrn_bwd_rev" if rev else "hgrn_bwd", grid=(nt,),
                  in_specs=in_specs, out_specs=out_specs, out_shape=out_shape,
                  scratch=[pltpu.VMEM((4, HGD, HGD), F32)],
                  aliases={len(args) - 1: 0}, carry=carry)(*args)


def _head_rms(o, w, nheads):
    outs = []
    for h in range(nheads):
        oh = o[:, h * HGD:(h + 1) * HGD]
        outs.append(oh * lax.rsqrt(jnp.mean(oh * oh, axis=-1, keepdims=True) + EPS))
    return jnp.concatenate(outs, axis=1)


def _readout_bwd(o0, o1, p, hw4, dy, dp, carry=None):
    tt = o0.shape[0]
    s_len = tt - L

    def body(o0_ref, o1_ref, g_ref, w_ref, dy_ref, _dp_in, dp_ref, do_ref, dw_ref):
        i = pl.program_id(0)

        @pl.when(i == 0)
        def _():
            dw_ref[...] = jnp.zeros_like(dw_ref)
            dp_ref[...] = jnp.zeros_like(dp_ref)

        @pl.when(i >= 1)
        def _():
            o = o0_ref[...] + o1_ref[...]
            g = g_ref[...]
            w = w_ref[...]
            sg = _sig(g)
            dy_ = dy_ref[...]
            dsw = dy_ * (g * sg)
            outs, xhs = [], []
            for h in range(4):
                sl = slice(h * HGD, (h + 1) * HGD)
                oh = o[:, sl]
                r = lax.rsqrt(jnp.mean(oh * oh, axis=-1, keepdims=True) + EPS)
                xh = oh * r
                dxh = dsw[:, sl] * w[:, sl]
                outs.append(r * (dxh - xh * jnp.mean(dxh * xh, axis=-1, keepdims=True)))
                xhs.append(xh)
            xh = jnp.concatenate(xhs, axis=1)
            do_ref[...] = jnp.concatenate(outs, axis=1)
            dp_ref[...] = (dy_ * xh * w * (sg * (1.0 + g * (1.0 - sg)))).astype(BF16)
            dw_ref[...] += jnp.sum(dsw * xh, axis=0, keepdims=True)

    tok = pl.BlockSpec((TM, HGW), lambda i: (i, 0))
    lat = pl.BlockSpec((TM, HGW), lambda i: (jnp.maximum(i - 1, 0), 0))
    return _pcall(body, name="readout_bwd", grid=(tt // TM,),
                  in_specs=[tok, tok, pl.BlockSpec((TM, HGW), lambda i: (i, C_GHG)), _full((1, HGW)), lat, ANY],
                  out_specs=[pl.BlockSpec((TM, HGW), lambda i: (i, C_GHG)), lat, _full((1, HGW))],
                  out_shape=[jax.ShapeDtypeStruct(dp.shape, BF16), jax.ShapeDtypeStruct((s_len, HGW), F32),
                             jax.ShapeDtypeStruct((1, HGW), F32)],
                  aliases={5: 0}, carry=carry)(o0, o1, p, hw4, dy, dp)


def _rope_tables(s_len):
    t = np.arange(s_len)
    inv = ROPE_THETA ** (-np.arange(0, 32, 2, dtype=np.float64) / 32)
    def half(pos):
        ang = pos[:, None].astype(np.float64) * inv[None, :]
        return (np.concatenate([np.cos(ang), np.cos(ang)], 1), np.concatenate([-np.sin(ang), np.sin(ang)], 1))
    cr, sr = half(t // GRID_W)
    cc, sc = half(t % GRID_W)
    cos = np.concatenate([cr, cc, cr, cc], 1)
    sin = np.concatenate([sr, sc, sr, sc], 1)
    cos = np.concatenate([np.ones((L, 128)), cos], 0)
    sin = np.concatenate([np.zeros((L, 128)), sin], 0)
    return jnp.asarray(cos, F32), jnp.asarray(sin, F32)


def _blockdiag(n, w):
    i = np.arange(n)
    return jnp.asarray((i[:, None] // w == i[None, :] // w) / float(w), F32)


def _dup_matrix():
    m = np.zeros((128, 512), np.float32)
    for g in range(2):
        for j in range(4):
            for dd in range(HDIM):
                m[64 * g + dd, 256 * g + 64 * j + dd] = 1.0
    return m


def _head_mean(x, blockdiag):
    return _dot(x, blockdiag, prec=lax.Precision.HIGH)


def _rot(x):
    n = x.shape[1]
    lane = lax.broadcasted_iota(jnp.int32, x.shape, 1)
    return jnp.where((lane % 32) < 16, pltpu.roll(x, n - 16, 1), pltpu.roll(x, 16, 1))


def _qk_prep(p, cos, sin, qnw8, knw2, bd512, bd128, dup):
    tt = p.shape[0]

    def body(q_ref, kv_ref, cos_ref, sin_ref, qw_ref, kw_ref, b5_ref, b1_ref, dup_ref,
             qr_ref, k4_ref, v4_ref):
        cos_, sin_ = cos_ref[...], sin_ref[...]
        q = q_ref[...]
        qn = q * lax.rsqrt(_head_mean(q * q, b5_ref[...]) + EPS) * qw_ref[...]
        cos4 = jnp.concatenate([cos_] * 4, axis=1)
        sin4 = jnp.concatenate([sin_] * 4, axis=1)
        qr_ref[...] = ((qn * cos4 + _rot(qn) * sin4) * (HDIM ** -0.5)).astype(BF16)
        kv = kv_ref[...]
        k, v = kv[:, :128], kv[:, 128:]
        kn = k * lax.rsqrt(_head_mean(k * k, b1_ref[...]) + EPS) * kw_ref[...]
        kr = kn * cos_ + _rot(kn) * sin_
        k4_ref[...] = _bdot(kr, dup_ref[...]).astype(BF16)
        v4_ref[...] = _bdot(v, dup_ref[...]).astype(BF16)

    row = lambda w, cb: pl.BlockSpec((TM, w), lambda i: (i, cb))
    out = jax.ShapeDtypeStruct((tt, ATW), BF16)
    return _pcall(body, name="qk_prep", grid=(tt // TM,),
                  in_specs=[row(ATW, C_QRAW), row(256, C_KV), row(128, 0), row(128, 0),
                            _full((1, ATW)), _full((1, 128)), _full((ATW, ATW)), _full((128, 128)),
                            _full((128, ATW))],
                  out_specs=[row(ATW, 0)] * 3, out_shape=[out] * 3)(
                      p, p, cos, sin, qnw8, knw2, bd512, bd128, dup)


def _attn_masks(i, nb):
    r = lax.broadcasted_iota(jnp.int32, (4 * BLK, 3 * BLK + L), 0) % BLK
    c = lax.broadcasted_iota(jnp.int32, (4 * BLK, 3 * BLK + L), 1)
    kpos = (i - 1) * BLK + c
    loc = (jnp.abs(c - BLK - r) <= BLK) & (kpos >= 0) & (kpos < nb * BLK)
    return loc | (c >= 3 * BLK)


def _stack_mask():
    r = lax.broadcasted_iota(jnp.int32, (4 * BLK, 256), 0)
    lane = lax.broadcasted_iota(jnp.int32, (4 * BLK, 256), 1)
    return (r // BLK) == (lane // HDIM)


def _stack_heads(xg, fill=0.0):
    x4 = jnp.concatenate([xg] * 4, axis=0)
    return jnp.where(_stack_mask(), x4, jnp.full_like(x4, fill))


def _unstack_heads(x4):
    out = jnp.where(_lane_mask(0), x4[0:BLK], 0.0)
    for j in range(1, 4):
        out = out + jnp.where(_lane_mask(j), x4[j * BLK:(j + 1) * BLK], 0.0)
    return out


def _per_head_rows(vals):
    return jnp.concatenate([jnp.broadcast_to(v, (BLK, 1)) for v in vals], axis=0)


def _lane_mask(j):
    lane = lax.broadcasted_iota(jnp.int32, (1, 256), 1)
    return (lane // HDIM) == j


def _attn_specs(nb):
    blk = lambda off: pl.BlockSpec((BLK, ATW), lambda i: (jnp.clip(i + off, 0, nb - 1) + 2, 0))
    ctx = pl.BlockSpec((L, ATW), lambda i: (0, 0))
    return blk, ctx


def _attn_fwd(qr, k4, v4, sinks, carry=None):
    tt = qr.shape[0]
    s_len = tt - L
    nb = s_len // BLK

    def body(sk_ref, q_ref, kp, ko, kn, kc, vp, vo, vn, vc, y_ref, lse_ref):
        i = pl.program_id(0)
        valid = _attn_masks(i, nb)
        q = q_ref[...]
        ys, lses = [], []
        for g in range(2):
            gs = slice(256 * g, 256 * g + 256)
            kcat = jnp.concatenate([kp[:, gs], ko[:, gs], kn[:, gs], kc[:, gs]], axis=0)
            vcat = jnp.concatenate([vp[:, gs], vo[:, gs], vn[:, gs], vc[:, gs]], axis=0)
            sink4 = _per_head_rows([sk_ref[4 * g + j] for j in range(4)])
            q4 = _stack_heads(q[:, gs])
            o_parts, l_parts = [], []
            for hp in range(2):
                rows = slice(2 * BLK * hp, 2 * BLK * (hp + 1))
                sink = sink4[rows]
                s = jnp.where(valid[rows], _dot(q4[rows], kcat, NT), -1e30)
                m = jnp.maximum(jnp.max(s, axis=-1, keepdims=True), sink)
                e = jnp.exp(s - m)
                den = jnp.sum(e, axis=-1, keepdims=True) + jnp.exp(sink - m)
                o_parts.append(_bdot(e * (1.0 / den), vcat))
                l_parts.append(jnp.broadcast_to(m + jnp.log(den), (2 * BLK, 256)))
            ys.append(_unstack_heads(jnp.concatenate(o_parts, axis=0)))
            lses.append(_unstack_heads(jnp.concatenate(l_parts, axis=0)))
        y_ref[...] = jnp.concatenate(ys, axis=1).astype(BF16)
        lse_ref[...] = jnp.concatenate(lses, axis=1)

    blk, ctx = _attn_specs(nb)
    out = pl.BlockSpec((BLK, ATW), lambda i: (i, 0))
    return _pcall(body, name="attn_fwd", grid=(nb,),
                  in_specs=[pl.BlockSpec(memory_space=pltpu.SMEM), blk(0),
                            blk(-1), blk(0), blk(1), ctx, blk(-1), blk(0), blk(1), ctx],
                  out_specs=[out, out],
                  out_shape=[jax.ShapeDtypeStruct((s_len, ATW), BF16),
                             jax.ShapeDtypeStruct((s_len, ATW), F32)], carry=carry)(
                      sinks, qr, k4, k4, k4, k4, v4, v4, v4, v4)


def _attn_bwd(qr, k4, v4, sinks, y, lse, dy, carry=None):
    tt = qr.shape[0]
    s_len = tt - L
    nb = s_len // BLK

    def body(sk_ref, q_ref, kp, ko, kn, kc, vp, vo, vn, vc, y_ref, lse_ref, dy_ref,
             dq_ref, dkw_ref, dvw_ref, dkc_ref, dvc_ref, dsk_ref):
        i = pl.program_id(0)

        @pl.when(i == 0)
        def _():
            dkc_ref[...] = jnp.zeros_like(dkc_ref)
            dvc_ref[...] = jnp.zeros_like(dvc_ref)
            dsk_ref[...] = jnp.zeros_like(dsk_ref)

        valid = _attn_masks(i, nb)
        q = q_ref[...]
        dy_ = dy_ref[...]
        dly = dy_ * y_ref[...].astype(F32)
        lse_ = lse_ref[...]
        dqs = []
        for g in range(2):
            gs = slice(256 * g, 256 * g + 256)
            kcat = jnp.concatenate([kp[:, gs], ko[:, gs], kn[:, gs], kc[:, gs]], axis=0)
            vcat = jnp.concatenate([vp[:, gs], vo[:, gs], vn[:, gs], vc[:, gs]], axis=0)
            q4 = _stack_heads(q[:, gs])
            dy4 = _stack_heads(dy_[:, gs]).astype(BF16)
            lse4 = jnp.max(_stack_heads(lse_[:, gs], fill=-1e30), axis=-1, keepdims=True)
            delta = jnp.sum(_stack_heads(dly[:, gs]), axis=-1, keepdims=True)
            sink = _per_head_rows([sk_ref[4 * g + j] for j in range(4)])
            pr = jnp.where(valid, jnp.exp(_dot(q4, kcat, NT) - lse4), 0.0)
            dsb = (pr * (_dot(dy4, vcat, NT) - delta)).astype(BF16)
            dsink = jnp.exp(sink - lse4) * delta
            for j in range(4):
                dsk_ref[4 * g + j:4 * g + j + 1, :] += jnp.broadcast_to(
                    -jnp.sum(dsink[j * BLK:(j + 1) * BLK], axis=0, keepdims=True), (1, 128))
            dqs.append(_unstack_heads(_dot(dsb, kcat)))
            dkg = _dot(dsb, q4, TN)
            dvg = _dot(pr.astype(BF16), dy4, TN)
            dkw_ref[0, :, gs] = dkg[:3 * BLK]
            dvw_ref[0, :, gs] = dvg[:3 * BLK]
            dkc_ref[:, gs] += dkg[3 * BLK:]
            dvc_ref[:, gs] += dvg[3 * BLK:]
        dq_ref[...] = jnp.concatenate(dqs, axis=1)

    blk, ctx = _attn_specs(nb)
    out = pl.BlockSpec((BLK, ATW), lambda i: (i, 0))
    win = pl.BlockSpec((1, 3 * BLK, ATW), lambda i: (i, 0, 0))
    acc = _full((L, ATW))
    return _pcall(body, name="attn_bwd", grid=(nb,),
                  in_specs=[pl.BlockSpec(memory_space=pltpu.SMEM), blk(0),
                            blk(-1), blk(0), blk(1), ctx, blk(-1), blk(0), blk(1), ctx, out, out, out],
                  out_specs=[out, win, win, acc, acc, _full((8, 128))],
                  out_shape=[jax.ShapeDtypeStruct((s_len, ATW), F32),
                             jax.ShapeDtypeStruct((nb, 3 * BLK, ATW), F32),
                             jax.ShapeDtypeStruct((nb, 3 * BLK, ATW), F32),
                             jax.ShapeDtypeStruct((L, ATW), F32), jax.ShapeDtypeStruct((L, ATW), F32),
                             jax.ShapeDtypeStruct((8, 128), F32)], carry=carry)(
                      sinks, qr, k4, k4, k4, k4, v4, v4, v4, v4, y, lse, dy)


def _attn_post(p, cos, sin, qnw8, knw2, bd512, bd128, dupt, dq, dkw, dvw, dkc, dvc, dp, carry=None):
    tt = p.shape[0]
    s_len = tt - L
    nb = s_len // BLK
    nctx = L // BLK

    def body(q_ref, kv_ref, cos_ref, sin_ref, qw_ref, kw_ref, b5_ref, b1_ref, dupt_ref,
             dq_ref, kwp, kwo, kwn, vwp, vwo, vwn, dkc_ref, dvc_ref, _dp_in,
             dp_ref, dqw_ref, dkw_ref):
        t = pl.program_id(0)
        j = t - nctx

        @pl.when(t == 0)
        def _():
            dqw_ref[...] = jnp.zeros_like(dqw_ref)
            dkw_ref[...] = jnp.zeros_like(dkw_ref)

        is_lat = t >= nctx
        cos_, sin_ = cos_ref[...], sin_ref[...]
        has_p = is_lat & (j >= 1)
        has_n = is_lat & (j <= nb - 2)
        dk4 = (jnp.where(is_lat, kwo[0], dkc_ref[...]) + jnp.where(has_p, kwp[0], 0.0)
               + jnp.where(has_n, kwn[0], 0.0))
        dv4 = (jnp.where(is_lat, vwo[0], dvc_ref[...]) + jnp.where(has_p, vwp[0], 0.0)
               + jnp.where(has_n, vwn[0], 0.0))
        dkr = _dot(dk4, dupt_ref[...], prec=HI)
        dv = _dot(dv4, dupt_ref[...], prec=HI)
        kv = kv_ref[...]
        k = kv[:, :128]
        kw = kw_ref[...]
        rk = lax.rsqrt(_head_mean(k * k, b1_ref[...]) + EPS)
        xk = k * rk
        dkn = dkr * cos_ + _rot(dkr * sin_)
        dxk = dkn * kw
        dk = rk * (dxk - xk * _head_mean(dxk * xk, b1_ref[...]))
        dkw_ref[...] += jnp.sum(dkn * xk, axis=0, keepdims=True)
        q = q_ref[...]
        qw = qw_ref[...]
        rq = lax.rsqrt(_head_mean(q * q, b5_ref[...]) + EPS)
        xq = q * rq
        cos4 = jnp.concatenate([cos_] * 4, axis=1)
        sin4 = jnp.concatenate([sin_] * 4, axis=1)
        dqr = jnp.where(is_lat, dq_ref[...], 0.0) * (HDIM ** -0.5)
        dqn = dqr * cos4 + _rot(dqr * sin4)
        dxq = dqn * qw
        dqraw = rq * (dxq - xq * _head_mean(dxq * xq, b5_ref[...]))
        dqw_ref[...] += jnp.sum(dqn * xq, axis=0, keepdims=True)
        dp_ref[...] = jnp.concatenate([dqraw, dk, dv], axis=1).astype(BF16)

    row = lambda w, cb: pl.BlockSpec((BLK, w), lambda t: (t, cb))
    lat = pl.BlockSpec((BLK, ATW), lambda t: (jnp.maximum(t - nctx, 0), 0))

    def part(off):
        return pl.BlockSpec((1, BLK, ATW), lambda t: (jnp.clip(t - nctx + off, 0, nb - 1), 1 - off, 0))

    cacc = pl.BlockSpec((BLK, ATW), lambda t: (jnp.minimum(t, nctx - 1), 0))
    return _pcall(body, name="attn_post", grid=(tt // BLK,),
                  in_specs=[row(ATW, C_QRAW), row(256, C_KV), row(128, 0), row(128, 0),
                            _full((1, ATW)), _full((1, 128)), _full((ATW, ATW)), _full((128, 128)),
                            _full((ATW, 128)), lat, part(-1), part(0), part(1), part(-1), part(0), part(1),
                            cacc, cacc, ANY],
                  out_specs=[pl.BlockSpec((BLK, 768), lambda t: (t, C_QKV)), _full((1, ATW)), _full((1, 128))],
                  out_shape=[jax.ShapeDtypeStruct(dp.shape, BF16), jax.ShapeDtypeStruct((1, ATW), F32),
                             jax.ShapeDtypeStruct((1, 128), F32)],
                  aliases={18: 0}, carry=carry)(p, p, cos, sin, qnw8, knw2, bd512, bd128, dupt,
                                   dq, dkw, dkw, dkw, dvw, dvw, dvw, dkc, dvc, dp)


def _branch_merge(y_hg, y_at, bh4, ba4, p):
    s_len = y_hg.shape[0]

    def body(yh_ref, ya_ref, bh_ref, ba_ref, gh_ref, ga_ref, ah_ref, aa_ref, m_ref):
        yh, ya = yh_ref[...], ya_ref[...]
        ah = jnp.concatenate([_bdot(yh, bh_ref[j]) for j in range(4)], axis=1)
        aa = jnp.concatenate([_bdot(ya, ba_ref[j]) for j in range(4)], axis=1)
        ah_ref[...] = ah.astype(BF16)
        aa_ref[...] = aa.astype(BF16)
        m_ref[...] = (_sig(gh_ref[...]) * ah + _sig(ga_ref[...]) * aa).astype(BF16)

    row = pl.BlockSpec((TM, D), lambda i: (i, 0))
    y = pl.BlockSpec((TM, HGW), lambda i: (i, 0))
    f = jax.ShapeDtypeStruct((s_len, D), BF16)
    return _pcall(body, name="branch_merge", grid=(s_len // TM,),
                  in_specs=[y, y, _full(bh4.shape), _full(ba4.shape),
                            pl.BlockSpec((TM, D), lambda i: (i + 1, 2)), pl.BlockSpec((TM, D), lambda i: (i + 1, 3))],
                  out_specs=[row, row, row],
                  out_shape=[f, f, jax.ShapeDtypeStruct((s_len, D), BF16)])(y_hg, y_at, bh4, ba4, p, p)


def _branch_bwd(dmh, dma, bh4, ba4, y_hg, y_at):
    s_len = dmh.shape[0]
    nk = s_len // TS
    ns = D // 4

    def body(dh_ref, da_ref, bh_ref, ba_ref, yh_ref, ya_ref, dyh_ref, dya_ref, gh_ref, ga_ref, acc_h, acc_a):
        t = pl.program_id(0)

        @pl.when(t == 0)
        def _():
            acc_h[...] = jnp.zeros_like(acc_h)
            acc_a[...] = jnp.zeros_like(acc_a)

        for d_ref, w_ref, y_ref, dy_ref, acc in ((dh_ref, bh_ref, yh_ref, dyh_ref, acc_h),
                                                 (da_ref, ba_ref, ya_ref, dya_ref, acc_a)):
            y = y_ref[...]
            dy = jnp.zeros((TS, HGW), F32)
            for j in range(4):
                dj = d_ref[:, j * ns:(j + 1) * ns]
                dy = dy + _bdot(dj, w_ref[j], NT)
                acc[j] += _bdot(y, dj, TN)
            dy_ref[...] = dy

        @pl.when(t == nk - 1)
        def _():
            gh_ref[...] = acc_h[...].astype(BF16)
            ga_ref[...] = acc_a[...].astype(BF16)

    dm = pl.BlockSpec((TS, D), lambda t: (t, 0))
    y = pl.BlockSpec((TS, HGW), lambda t: (t, 0))
    w = _full(bh4.shape)
    fy = jax.ShapeDtypeStruct((s_len, HGW), F32)
    gw = jax.ShapeDtypeStruct(bh4.shape, BF16)
    return _pcall(body, name="branch_bwd", grid=(nk,), in_specs=[dm, dm, w, w, y, y],
                  out_specs=[y, y, w, w], out_shape=[fy, fy, gw, gw],
                  scratch=[pltpu.VMEM(bh4.shape, F32)] * 2)(dmh, dma, bh4, ba4, y_hg, y_at)


def _merge_bwd(dattn, w_o, mixed, ah, aa, p, carry=None):
    tt = p.shape[0]
    s_len = tt - L
    nt = tt // TM

    def body(da_ref, wo_ref, mx_ref, ah_ref, aa_ref, gh_ref, ga_ref, dp_ref, dmh_ref, dma_ref, go_ref, acc):
        i = pl.program_id(0)

        @pl.when(i == 0)
        def _():
            dp_ref[...] = jnp.zeros_like(dp_ref)
            acc[...] = jnp.zeros_like(acc)

        @pl.when(i >= 1)
        def _():
            da = da_ref[...]
            acc[...] += _bdot(mx_ref[...], da, TN)
            dm_ = _bdot(da, wo_ref[...], NT)
            sh, sa = _sig(gh_ref[...]), _sig(ga_ref[...])
            dp_ref[...] = jnp.concatenate([dm_ * ah_ref[...].astype(F32) * sh * (1.0 - sh),
                                           dm_ * aa_ref[...].astype(F32) * sa * (1.0 - sa)], axis=1).astype(BF16)
            dmh_ref[...] = (dm_ * sh).astype(BF16)
            dma_ref[...] = (dm_ * sa).astype(BF16)

        @pl.when(i == nt - 1)
        def _():
            go_ref[...] = acc[...].astype(BF16)

    lat = pl.BlockSpec((TM, D), lambda i: (jnp.maximum(i - 1, 0), 0))
    return _pcall(body, name="merge_bwd", grid=(nt,),
                  in_specs=[lat, _full((D, D)), lat, lat, lat, pl.BlockSpec((TM, D), lambda i: (i, 2)),
                            pl.BlockSpec((TM, D), lambda i: (i, 3))],
                  out_specs=[pl.BlockSpec((TM, 2 * D), lambda i: (i, C_GATES)), lat, lat, _full((D, D))],
                  out_shape=[jax.ShapeDtypeStruct((tt, NCOL), BF16), jax.ShapeDtypeStruct((s_len, D), BF16),
                             jax.ShapeDtypeStruct((s_len, D), BF16), jax.ShapeDtypeStruct((D, D), BF16)],
                  scratch=[pltpu.VMEM((D, D), F32)], carry=carry)(dattn, w_o, mixed, ah, aa, p, p)


def _local_step(x, ctx, tgt, mod, modc, nw1, nw2, lg, hw, qnw, knw, sinks,
                w_in, wts, dist=None):
    s_len = x.shape[0]
    tt = s_len + L
    ss1 = jnp.stack([modc, mod[0:2]])
    ss2 = mod[3:5][None]
    g1, g2 = mod[2:3], mod[5:6]
    hw4 = jnp.tile(hw, (1, 4))
    qnw8 = jnp.tile(qnw, (1, 8))
    knw2 = jnp.tile(knw, (1, 2))
    cos, sin = _rope_tables(s_len)
    bd512, bd128 = _blockdiag(ATW, HDIM), _blockdiag(128, HDIM)
    dupm = _dup_matrix()
    dup, dupt = jnp.asarray(dupm, BF16), jnp.asarray(dupm.T, F32)
    tmt = tt

    def four(b):
        return b.reshape(4, 2 * b.shape[1], b.shape[2])

    def halves(g):
        return g.reshape(4, 2, g.shape[1] // 2, g.shape[2])

    h = _mod1(ctx, x, nw1, ss1)
    if dist is None:
        bh4, ba4, w_o, g4, u4, dn4 = wts
        p = _mm_in(h, w_in, tmt)
        o0, st0 = _hgrn_fwd(p, lg, rev=False)
        o1, st1, y_hg = _hgrn_fwd(p, lg, rev=True, readout=(o0, hw4))
    else:
        core, chip = dist
        half = wts[3].shape[1] // 2
        p, first = _mm_in(h, w_in, tmt, carry=_carry_join(_carry_gather(list(wts[0:3])),
                                                          _carry_gather([wts[3]], rows=[(0, half)])))
        (o0, st0), (g8,) = _hgrn_fwd(p, lg, rev=False, carry=_carry_gather([first[3]], rows=[(half, half)]))
        (o1, st1, y_hg), (dn8a,) = _hgrn_fwd(p, lg, rev=True, readout=(o0, hw4),
                                             carry=_carry_gather([wts[5]], rows=[(0, half)]))
        bh4, ba4, w_o, g4 = four(first[0]), four(first[1]), four(first[2]).reshape(D, D), four(g8)
    qr, k4, v4 = _qk_prep(p, cos, sin, qnw8, knw2, bd512, bd128, dup)
    if dist is None:
        y_at, lse = _attn_fwd(qr, k4, v4, sinks)
    else:
        (y_at, lse), (u8, dn8) = _attn_fwd(qr, k4, v4, sinks,
                                           carry=_carry_gather([wts[4], dn8a], rows=[None, (half, half)]))
        u4, dn4 = four(u8), four(dn8)
    ah, aa, mixed = _branch_merge(y_hg, y_at, bh4, ba4, p)
    ao, x1, h2 = _out_proj_mod2(mixed, w_o, x, g1, nw2, ss2)
    a4, b4, z4 = _ffn_up(h2, g4, u4)
    sq, dx2, dyb, dg2 = _ffn_down_loss(z4, dn4, x1, g2, tgt)

    da4, db4 = _ffn_dz(dyb, dn4, a4, b4)
    g_dn = _ffn_gdn(z4, dyb)
    if dist is None:
        dh2 = _ffn_dh2(da4, db4, g4, u4)
    else:
        dn_units = [halves(g_dn)]
        dh2, dn_recv = _ffn_dh2(da4, db4, g4, u4, carry=_carry_pairx(dn_units))
        dn_pairs = _rs_pair_add(dn_units, dn_recv, core)
    if dist is None:
        g_g, g_u = _ffn_ggu(h2, da4, db4)
    else:
        (g_g, g_u), c_dn = _ffn_ggu(h2, da4, db4, carry=_carry_chipx(dn_pairs))
        red_dn = _rs_chip_add(dn_pairs, c_dn, core, chip)
    dx1, dattn, dss2, dnw2, dg1 = _mod2_bwd(x1, dh2, dx2, ao, nw2, ss2, g1)
    if dist is None:
        dp, dmh, dma, g_o = _merge_bwd(dattn, w_o, mixed, ah, aa, p)
    else:
        gu_units = [halves(g_g), halves(g_u)]
        (dp, dmh, dma, g_o), gu_recv = _merge_bwd(dattn, w_o, mixed, ah, aa, p, carry=_carry_pairx(gu_units))
        ffn_pairs = list(dn_pairs) + list(_rs_pair_add(gu_units, gu_recv, core))
    dy_hg, dy_at, g_bh, g_ba = _branch_bwd(dmh, dma, bh4, ba4, y_hg, y_at)
    if dist is None:
        dp, do, dhw4 = _readout_bwd(o0, o1, p, hw4, dy_hg, dp)
        dq, dkw, dvw, dkc, dvc, dsk = _attn_bwd(qr, k4, v4, sinks, y_at, lse, dy_at)
        dp, dqnw8, dknw2 = _attn_post(p, cos, sin, qnw8, knw2, bd512, bd128, dupt, dq, dkw, dvw, dkc, dvc, dp)
    else:
        mix_units = [halves(g_bh), halves(g_ba), halves(g_o.reshape(4, D // 4, D))]
        (dp, do, dhw4), mix_recv = _readout_bwd(o0, o1, p, hw4, dy_hg, dp, carry=_carry_pairx(mix_units))
        mix_pairs = _rs_pair_add(mix_units, mix_recv, core)
        (dq, dkw, dvw, dkc, dvc, dsk), bwd = _attn_bwd(
            qr, k4, v4, sinks, y_at, lse, dy_at,
            carry=_carry_join(_carry_chipx(ffn_pairs[1:2]), _carry_sibx(red_dn)))
        red_g = _rs_chip_add(ffn_pairs[1:2], bwd[0:1], core, chip)
        (dp, dqnw8, dknw2), post = _attn_post(
            p, cos, sin, qnw8, knw2, bd512, bd128, dupt, dq, dkw, dvw, dkc, dvc, dp, carry=_carry_sibx(red_g))
    if dist is None:
        dp, dv0, dq0, dlg0 = _hgrn_bwd(p, lg, do, st0, dp, None, rev=False)
        dp, dlg1 = _hgrn_bwd(p, lg, do, st1, dp, (dv0, dq0), rev=True)
    else:
        (dp, dv0, dq0, dlg0), c_u = _hgrn_bwd(p, lg, do, st0, dp, None, rev=False,
                                              carry=_carry_chipx(ffn_pairs[2:3]))
        red_u = _rs_chip_add(ffn_pairs[2:3], c_u, core, chip)
        (dp, dlg1), last = _hgrn_bwd(p, lg, do, st1, dp, (dv0, dq0), rev=True,
                                     carry=_carry_join(_carry_chipx(mix_pairs), _carry_sibx(red_u)))
        mix_reds = _rs_chip_add(mix_pairs, last[0:3], core, chip)
        ffn_done = bwd[1:2] + post[0:1] + last[3:4]
    g_in = _mm_gin(dp, h, tmt)
    if dist is None:
        dh = _mm_dh(dp, w_in, tmt)
        gx, dss1, dnw1 = _mod1_bwd(ctx, x, dh, dx1, nw1, ss1)
        rs = None
    else:
        in_units = [halves(g_in.reshape(4, NCOL // 4, D))]
        in_pairs = _rs_pair_add(in_units, _pair_exchange(in_units), core)
        first_rows, rest_rows = _split_rows(in_pairs[0].shape[1], IN_ROWS_WITH_DH)
        dh, both = _mm_dh(dp, w_in, tmt, carry=_carry_join(_carry_chipx(in_pairs, rows=first_rows),
                                                           _carry_sibx(mix_reds)))
        in_part, mix_done = both[0:1], both[1:4]
        gx, dss1, dnw1 = _mod1_bwd(ctx, x, dh, dx1, nw1, ss1)
        rs = dict(ffn_done=ffn_done, mix_done=mix_done, in_pairs=in_pairs, in_part=in_part, rest_rows=rest_rows)

    dmod = jnp.concatenate([dss1[1], dg1, dss2, dg2], axis=0)
    dmodc = dss1[0]
    raw = (dss1, dg1, dss2, dg2, dnw1, dnw2, dhw4, dqnw8, dknw2, dsk, dlg0, dlg1)
    small = dict(raw=raw, dmod=dmod, dmodc=dmodc, dnw1=dnw1, dnw2=dnw2,
                 dhw=dhw4.reshape(4, HGD).sum(0, keepdims=True),
                 dqnw=dqnw8.reshape(8, HDIM).sum(0, keepdims=True),
                 dknw=dknw2.reshape(2, HDIM).sum(0, keepdims=True),
                 dsinks=dsk[:, 0], dlg=jnp.concatenate([dlg0, dlg1], axis=0))
    big = dict(w_in=g_in, w_bh=g_bh, w_ba=g_ba, w_o=g_o, w_g=g_g, w_u=g_u, w_dn=g_dn)
    return sq, gx, big, small, rs


def _place():
    x, y, c = lax.axis_index("x"), lax.axis_index("y"), lax.axis_index("c")
    return x, y, c


def _gather_blocks(x_refs, out_refs, send_sems, recv_sems, local_sems):
    n = len(out_refs)
    x, y, c = _place()
    me, sibling = (x, y, c), (x, y, 1 - c)
    chips = [(1 - x, y), (x, 1 - y), (1 - x, 1 - y)]

    def slot(u, px, py, pc):
        return out_refs[u].at[4 * px + 2 * py + pc]

    def copy(u, k, block, to, src=None):
        return pltpu.make_async_remote_copy(
            src_ref=slot(u, *block) if src is None else src, dst_ref=slot(u, *block),
            send_sem=send_sems.at[u, k], recv_sem=recv_sems.at[u, k], device_id=to, device_id_type=MESH)

    mines = [pltpu.make_async_copy(x_refs[u], slot(u, *me), local_sems.at[u]) for u in range(n)]
    for cp in mines:
        cp.start()
    first = []
    for u in range(n):
        first.append(copy(u, 0, me, sibling, src=x_refs[u]))
        first += [copy(u, 1 + j, me, (*chip, c), src=x_refs[u]) for j, chip in enumerate(chips)]
    for cp in first:
        cp.start()
    passed = []
    for j, chip in enumerate(chips):
        for u in range(n):
            copy(u, 1 + j, (*chip, c), me).wait_recv()
            fwd = copy(u, 4 + j, (*chip, c), sibling)
            fwd.start()
            passed.append(fwd)
    for u in range(n):
        copy(u, 0, sibling, me).wait_recv()
    for j, chip in enumerate(chips):
        for u in range(n):
            copy(u, 4 + j, (*chip, 1 - c), me).wait_recv()
    for cp in first + passed:
        cp.wait_send()
    for cp in mines:
        cp.wait()


def _gather_sems(n):
    return [pltpu.SemaphoreType.DMA((n, 7)), pltpu.SemaphoreType.DMA((n, 7)), pltpu.SemaphoreType.DMA((n,))]


def _allgather(blks, *, name):
    n = len(blks)
    vm = pl.BlockSpec(memory_space=pltpu.VMEM)

    def body(*refs):
        _peer_barrier("both")
        _gather_blocks(refs[:n], refs[n:2 * n], *refs[2 * n:])

    return pl.pallas_call(
        body, name=name, out_shape=[jax.ShapeDtypeStruct((8,) + b.shape, b.dtype) for b in blks],
        in_specs=[vm] * n, out_specs=[vm] * n, scratch_shapes=_gather_sems(n),
        compiler_params=pltpu.CompilerParams(collective_id=BARRIER_IDS["both"]))(*blks)


def _cast_place(ws, c, dev):
    n = len(ws)

    def body(s_ref, *refs):
        for u in range(n):
            refs[n + u][0] = refs[u][...].astype(BF16)

    in_specs, out_specs, out_shape = [], [], []
    for w in ws:
        q, cols = w.shape[0] // 4, w.shape[1]
        in_specs.append(pl.BlockSpec((q, cols), lambda i, s: (2 * s[0] + i, 0)))
        out_specs.append(pl.BlockSpec((1, q, cols), lambda i, s: (s[1], i, 0)))
        out_shape.append(jax.ShapeDtypeStruct((8, 2 * q, cols), BF16))
    return pl.pallas_call(
        body, name="cast_place",
        grid_spec=pltpu.PrefetchScalarGridSpec(num_scalar_prefetch=1, grid=(2,), in_specs=in_specs,
                                               out_specs=out_specs),
        out_shape=_out_hbm(out_shape),
        compiler_params=pltpu.CompilerParams(vmem_limit_bytes=48 << 20))(jnp.stack([c, dev]), *_in_hbm(ws))


def _gather_phases(out_refs, send_sems, recv_sems, rows=None):
    n = len(out_refs)
    x, y, c = _place()
    me, sibling = (x, y, c), (x, y, 1 - c)
    chips = [(1 - x, y), (x, 1 - y), (1 - x, 1 - y)]

    def copy(u, k, block, to):
        px, py, pc = block
        ref = out_refs[u].at[4 * px + 2 * py + pc]
        if rows is not None and rows[u] is not None:
            ref = ref.at[pl.ds(rows[u][0], rows[u][1])]
        return pltpu.make_async_remote_copy(src_ref=ref, dst_ref=ref, send_sem=send_sems.at[u, k],
                                            recv_sem=recv_sems.at[u, k], device_id=to, device_id_type=MESH)

    def start():
        for u in range(n):
            copy(u, 0, me, sibling).start()
            for j, chip in enumerate(chips):
                copy(u, 1 + j, me, (*chip, c)).start()

    def mid():
        for j, chip in enumerate(chips):
            for u in range(n):
                copy(u, 1 + j, (*chip, c), me).wait_recv()
                copy(u, 4 + j, (*chip, c), sibling).start()

    def end():
        for u in range(n):
            copy(u, 0, sibling, me).wait_recv()
        for j, chip in enumerate(chips):
            for u in range(n):
                copy(u, 4 + j, (*chip, 1 - c), me).wait_recv()
        for u in range(n):
            copy(u, 0, me, sibling).wait_send()
            for j, chip in enumerate(chips):
                copy(u, 1 + j, me, (*chip, c)).wait_send()
                copy(u, 4 + j, (*chip, c), sibling).wait_send()

    return start, mid, end


def _carry_gather(bufs, rows=None):
    n = len(bufs)
    return _Carry(bufs, [jax.ShapeDtypeStruct(b.shape, b.dtype) for b in bufs], {u: u for u in range(n)},
                  [pltpu.SemaphoreType.DMA((n, 7)), pltpu.SemaphoreType.DMA((n, 7))],
                  lambda ins, outs, sems: _gather_phases(outs, *sems, rows=rows), "both")


def _ag_small(raw, sq):
    def body(dss1, dg1, dss2, dg2, dnw1, dnw2, dhw4, dqnw8, dknw2, dsk, dlg0, dlg1, sq_ref,
             out_ref, tot_ref, blk, send_sems, recv_sems, local_sems):
        _peer_barrier("both")
        blk[...] = jnp.zeros_like(blk)
        blk[0:2, :] = dss1[1]
        blk[2:3, :] = dg1[...]
        blk[3:5, :] = dss2[...]
        blk[5:6, :] = dg2[...]
        blk[6:8, :] = dss1[0]
        blk[8:9, :] = dnw1[...]
        blk[9:10, :] = dnw2[...]
        blk[10:11, 0:HGW] = dhw4[...]
        blk[10:11, HGW:D] = dqnw8[...]
        blk[11:12, 0:128] = dknw2[...]
        blk[12:14, 0:HGW] = dlg0[0]
        blk[14:16, 0:HGW] = dlg1[0]
        blk[16:24, 0:128] = dsk[...]
        blk[24:25, :] = sq_ref[...]
        _gather_blocks([blk], [out_ref], send_sems, recv_sems, local_sems)
        acc = out_ref[0]
        for i in range(1, 8):
            acc = acc + out_ref[i]
        tot_ref[...] = acc

    vm = pl.BlockSpec(memory_space=pltpu.VMEM)
    return pl.pallas_call(
        body, name="ag_small",
        out_shape=[jax.ShapeDtypeStruct((8, 32, D), F32), jax.ShapeDtypeStruct((32, D), F32)],
        in_specs=[vm] * 13, out_specs=[vm, vm],
        scratch_shapes=[pltpu.VMEM((32, D), F32)] + _gather_sems(1),
        compiler_params=pltpu.CompilerParams(collective_id=BARRIER_IDS["both"]))(*raw, sq)


def _pairx_shapes(units):
    return [jax.ShapeDtypeStruct((4,) + g.shape[2:], g.dtype) for g in units]


def _pairx_phases(g_refs, r_refs, send_sems, recv_sems):
    n = len(g_refs)
    x, y, c = _place()
    cps = [pltpu.make_async_remote_copy(
        src_ref=g_refs[u].at[j, 1 - c], dst_ref=r_refs[u].at[j], send_sem=send_sems.at[u, j],
        recv_sem=recv_sems.at[u, j], device_id=(x, y, 1 - c), device_id_type=MESH)
        for u in range(n) for j in range(4)]

    def start():
        for cp in cps:
            cp.start()

    def end():
        for cp in cps:
            cp.wait()

    return start, None, end


def _carry_pairx(units):
    n = len(units)
    return _Carry(units, _pairx_shapes(units), {},
                  [pltpu.SemaphoreType.DMA((n, 4)), pltpu.SemaphoreType.DMA((n, 4))],
                  lambda ins, outs, sems: _pairx_phases(ins, outs, *sems), "sib")


def _pair_exchange(units):
    n = len(units)

    def body(*refs):
        _peer_barrier("sib")
        start, _, end = _pairx_phases(refs[:n], refs[n:2 * n], *refs[2 * n:])
        start()
        end()

    return pl.pallas_call(
        body, name="pair_exchange", out_shape=_pairx_shapes(units), in_specs=[ANY] * n, out_specs=[ANY] * n,
        scratch_shapes=[pltpu.SemaphoreType.DMA((n, 4))] * 2,
        compiler_params=pltpu.CompilerParams(collective_id=BARRIER_IDS["sib"]))(*units)


IN_ROWS_WITH_DH = 0.6


def _split_rows(h, share):
    first = int(h * share) // BF16_SUBLANES * BF16_SUBLANES
    return (0, first), (first, h - first)


def _rs_pair_add(units, recvs, c):
    n = len(units)

    def body(c_ref, *refs):
        for u in range(n):
            refs[2 * n + u][...] = (refs[u][0].astype(F32) + refs[n + u][...].astype(F32)).astype(BF16)

    in_specs, out_specs, out_shape = [], [], []
    for g in units:
        h, w = g.shape[2] // 2, g.shape[3]
        in_specs.append(pl.BlockSpec((1, 1, h, w), lambda j, i, cr: (j, cr[0], i, 0)))
    for g in units:
        h, w = g.shape[2] // 2, g.shape[3]
        in_specs.append(pl.BlockSpec((1, h, w), lambda j, i, cr: (j, i, 0)))
        out_specs.append(pl.BlockSpec((1, h, w), lambda j, i, cr: (j, i, 0)))
        out_shape.append(jax.ShapeDtypeStruct((4, 2 * h, w), BF16))
    return pl.pallas_call(
        body, name="rs_pair_add",
        grid_spec=pltpu.PrefetchScalarGridSpec(num_scalar_prefetch=1, grid=(4, 2), in_specs=in_specs,
                                               out_specs=out_specs),
        out_shape=_out_hbm(out_shape),
        compiler_params=pltpu.CompilerParams(vmem_limit_bytes=48 << 20))(
            c.reshape(1), *_in_hbm(list(units) + list(recvs)))


def _chipx_phases(p_refs, r_refs, send_sems, recv_sems, rows=None):
    n = len(p_refs)
    x, y, c = _place()
    k = 2 * x + y

    def part(ref):
        return ref if rows is None else ref.at[pl.ds(rows[0], rows[1])]

    sends = []
    for d in range(1, 4):
        j = (k + d) % 4
        for u in range(n):
            sends.append(pltpu.make_async_remote_copy(
                src_ref=part(p_refs[u].at[j]), dst_ref=part(r_refs[u].at[k]), send_sem=send_sems.at[u, d - 1],
                recv_sem=recv_sems.at[u, d - 1], device_id=(j // 2, j % 2, c), device_id_type=MESH))

    def start():
        for cp in sends:
            cp.start()

    def end():
        for d in range(1, 4):
            src = (k + 4 - d) % 4
            for u in range(n):
                pltpu.make_async_remote_copy(
                    src_ref=part(p_refs[u].at[src]), dst_ref=part(r_refs[u].at[src]),
                    send_sem=send_sems.at[u, d - 1], recv_sem=recv_sems.at[u, d - 1], device_id=(x, y, c),
                    device_id_type=MESH).wait_recv()
        for cp in sends:
            cp.wait_send()

    return start, None, end


def _carry_chipx(pairs, rows=None, into=None):
    n = len(pairs)
    sems = [pltpu.SemaphoreType.DMA((n, 3)), pltpu.SemaphoreType.DMA((n, 3))]
    shapes = [jax.ShapeDtypeStruct(p.shape, p.dtype) for p in pairs]
    if into is None:
        return _Carry(pairs, shapes, {}, sems, lambda ins, outs, s: _chipx_phases(ins, outs, *s, rows=rows),
                      "chips")
    return _Carry(list(pairs) + list(into), shapes, {n + u: u for u in range(n)}, sems,
                  lambda ins, outs, s: _chipx_phases(ins[:n], outs, *s, rows=rows), "chips")


def _rs_chip_add(pairs, contribs, c, chip):
    n = len(pairs)

    def body(s_ref, *refs):
        for u in range(n):
            a, b, c_, d = refs[4 * u:4 * u + 4]
            refs[4 * n + u][0] = ((a[0].astype(F32) + b[0].astype(F32)) + c_[0].astype(F32)) + d[0].astype(F32)

    in_specs, out_specs, out_shape, args = [], [], [], []
    for p, r in zip(pairs, contribs):
        h, w = p.shape[1] // 2, p.shape[2]
        in_specs += [pl.BlockSpec((1, h, w), functools.partial(lambda d, i, s: ((s[1] + d) % 4, i, 0), d))
                     for d in range(4)]
        args += [p, r, r, r]
        out_specs.append(pl.BlockSpec((1, h, w), lambda i, s: (s[0], i, 0)))
        out_shape.append(jax.ShapeDtypeStruct((2, 2 * h, w), F32))
    return pl.pallas_call(
        body, name="rs_chip_add",
        grid_spec=pltpu.PrefetchScalarGridSpec(num_scalar_prefetch=1, grid=(2,), in_specs=in_specs,
                                               out_specs=out_specs),
        out_shape=_out_hbm(out_shape),
        compiler_params=pltpu.CompilerParams(vmem_limit_bytes=48 << 20))(jnp.stack([c, chip]), *_in_hbm(args))


def _rs_sibling_gather(reds):
    n = len(reds)

    def body(*refs):
        _peer_barrier("sib")
        start, _, end = _sibx_phases(refs[n:2 * n], *refs[2 * n:])
        start()
        end()

    return pl.pallas_call(
        body, name="rs_sibling_gather", out_shape=[jax.ShapeDtypeStruct(r.shape, r.dtype) for r in reds],
        in_specs=[ANY] * n, out_specs=[ANY] * n, input_output_aliases={u: u for u in range(n)},
        scratch_shapes=[pltpu.SemaphoreType.DMA((n,))] * 2,
        compiler_params=pltpu.CompilerParams(collective_id=BARRIER_IDS["sib"]))(*reds)


def _sibx_phases(o_refs, send_sems, recv_sems):
    n = len(o_refs)
    x, y, c = _place()
    cps = [pltpu.make_async_remote_copy(
        src_ref=o_refs[u].at[c], dst_ref=o_refs[u].at[c], send_sem=send_sems.at[u], recv_sem=recv_sems.at[u],
        device_id=(x, y, 1 - c), device_id_type=MESH) for u in range(n)]

    def start():
        for cp in cps:
            cp.start()

    def end():
        for u in range(n):
            cps[u].wait_send()
            pltpu.make_async_remote_copy(
                src_ref=o_refs[u].at[1 - c], dst_ref=o_refs[u].at[1 - c], send_sem=send_sems.at[u],
                recv_sem=recv_sems.at[u], device_id=(x, y, 1 - c), device_id_type=MESH).wait_recv()

    return start, None, end


def _carry_sibx(reds):
    n = len(reds)
    return _Carry(reds, [jax.ShapeDtypeStruct(r.shape, r.dtype) for r in reds], {u: u for u in range(n)},
                  [pltpu.SemaphoreType.DMA((n,))] * 2, lambda ins, outs, sems: _sibx_phases(outs, *sems), "sib")


def _prologue(blk, c_ctx, w, b, in8):
    n = w.shape[1]

    def body(blk_ref, cctx_ref, w_ref, b_ref, _in_in, g0_ref, c16_ref, g1_ref, in_ref, mod_s,
             s1, r1, l1, s2, r2, l2, s3, r3):
        _peer_barrier("both")
        start, mid, end = _gather_phases([in_ref], s3, r3)
        _gather_blocks([blk_ref], [g0_ref], s1, r1, l1)
        start()
        c16 = jnp.concatenate([g0_ref[i, 0:1, :] for i in range(8)] + [cctx_ref[...], jnp.zeros((7, D), F32)],
                              axis=0)
        c16_ref[...] = c16
        mod_s[...] = _dot(c16 * _sig(c16), w_ref[...], prec=HI) + b_ref[...]
        _gather_blocks([mod_s], [g1_ref], s2, r2, l2)
        mid()
        end()

    vm = pl.BlockSpec(memory_space=pltpu.VMEM)
    return pl.pallas_call(
        body, name="prologue",
        out_shape=[jax.ShapeDtypeStruct((8, 8, D), F32), jax.ShapeDtypeStruct((16, D), F32),
                   jax.ShapeDtypeStruct((8, 16, n), F32), jax.ShapeDtypeStruct(in8.shape, in8.dtype)],
        in_specs=[vm, vm, vm, vm, ANY], out_specs=[vm, vm, vm, ANY], input_output_aliases={4: 3},
        scratch_shapes=[pltpu.VMEM((16, n), F32)] + _gather_sems(1) + _gather_sems(1)
        + [pltpu.SemaphoreType.DMA((1, 7)), pltpu.SemaphoreType.DMA((1, 7))],
        compiler_params=pltpu.CompilerParams(vmem_limit_bytes=48 << 20,
                                             collective_id=BARRIER_IDS["both"]))(blk, c_ctx, w, b, in8)


def _ada_bwd(c16, dmod16, w, carry=None):
    n = w.shape[1]
    tn = 512

    def body(c_ref, d_ref, w_ref, gw_ref, gc_ref):
        j = pl.program_id(0)

        @pl.when(j == 0)
        def _():
            gc_ref[...] = jnp.zeros_like(gc_ref)

        cc = c_ref[...]
        dm = d_ref[...]
        gw_ref[...] = _dot(cc * _sig(cc), dm, TN, prec=HI)
        gc_ref[...] += _dot(dm, w_ref[...], NT, prec=HI)

    return _pcall(body, name="ada_bwd", grid=(n // tn,),
                  in_specs=[_full((16, D)), pl.BlockSpec((16, tn), lambda j: (0, j)),
                            pl.BlockSpec((D, tn), lambda j: (0, j))],
                  out_specs=[pl.BlockSpec((D, tn), lambda j: (0, j)), _full((16, D))],
                  out_shape=[jax.ShapeDtypeStruct((D, n), F32),
                             jax.ShapeDtypeStruct((16, D), F32)], carry=carry)(c16, dmod16, w)


def _adam_math(w, g, m, v):
    c1 = 1.0 - ADAM_B1 ** ADAM_STEP
    c2 = 1.0 - ADAM_B2 ** ADAM_STEP
    nm = ADAM_B1 * m + (1.0 - ADAM_B1) * g
    nv = ADAM_B2 * v + (1.0 - ADAM_B2) * (g * g)
    return -ADAM_LR * ((nm / c1) / (jnp.sqrt(nv / c2) + ADAM_EPS) + ADAM_WD * w), nm, nv


def _adamw_small(ws, gs, ms, vs):
    n = len(ws)

    def body(*refs):
        for u in range(n):
            d_, nm, nv = _adam_math(refs[u][...], refs[n + u][...], refs[2 * n + u][...], refs[3 * n + u][...])
            refs[4 * n + u][...] = d_
            refs[5 * n + u][...] = nm
            refs[6 * n + u][...] = nv

    specs = [_full(w.shape) for w in ws]
    shapes = [jax.ShapeDtypeStruct(w.shape, F32) for w in ws]
    out = _pcall(body, name="adamw_small", grid=(1,), in_specs=specs * 4, out_specs=specs * 3,
                 out_shape=shapes * 3)(*ws, *gs, *ms, *vs)
    return out[:n], out[n:2 * n], out[2 * n:]


def _cctx_grad(parts, c_ctx):
    def body(p_ref, c_ref, o_ref):
        acc = p_ref[0:1, :]
        for k in range(1, 4):
            acc = acc + p_ref[k:k + 1, :]
        cc = c_ref[...]
        s = _sig(cc)
        o_ref[...] = acc * (s * (1.0 + cc * (1.0 - s)))

    return _pcall(body, name="cctx_grad", grid=(1,), in_specs=[_full(parts.shape), _full((1, D))],
                  out_specs=_full((1, D)), out_shape=jax.ShapeDtypeStruct((1, D), F32))(parts, c_ctx)


ADAM_STEPS = 8


def _adamw_multi(ws, gs, ms, vs, *, name, carry=None):
    n = len(ws)

    def body(*refs):
        for u in range(n):
            refs[4 * n + u][...], refs[5 * n + u][...], refs[6 * n + u][...] = _adam_math(
                refs[u][...], refs[n + u][...], refs[2 * n + u][...], refs[3 * n + u][...])

    specs = [pl.BlockSpec((w.shape[0] // ADAM_STEPS, w.shape[1]), lambda i: (i, 0)) for w in ws]
    shapes = [jax.ShapeDtypeStruct(w.shape, F32) for w in ws]
    res = _pcall(body, name=name, grid=(ADAM_STEPS,), in_specs=specs * 4, out_specs=specs * 3,
                 out_shape=shapes * 3, carry=carry)(*ws, *gs, *ms, *vs)
    out, extra = res if carry is not None else (res, None)
    return (out[:n], out[n:2 * n], out[2 * n:]), extra


def kernel(x, c, ctx, c_ctx, w_ada, b_ada, norm_mix_w, norm_ffn_w, w_in, hgrn_lb_logits, hgrn_norm_w, q_norm_w, k_norm_w, attn_sinks, w_branch_hgrn, w_branch_attn, w_out, w_ffn_gate, w_ffn_up, w_ffn_down, loss_target, m_c_ctx, m_w_ada, m_b_ada, m_norm_mix_w, m_norm_ffn_w, m_w_in, m_hgrn_lb_logits, m_hgrn_norm_w, m_q_norm_w, m_k_norm_w, m_attn_sinks, m_w_branch_hgrn, m_w_branch_attn, m_w_out, m_w_ffn_gate, m_w_ffn_up, m_w_ffn_down, v_c_ctx, v_w_ada, v_b_ada, v_norm_mix_w, v_norm_ffn_w, v_w_in, v_hgrn_lb_logits, v_hgrn_norm_w, v_q_norm_w, v_k_norm_w, v_attn_sinks, v_w_branch_hgrn, v_w_branch_attn, v_w_out, v_w_ffn_gate, v_w_ffn_up, v_w_ffn_down):
    xi, yi, ci = _place()
    chip = 2 * xi + yi
    dev = 2 * chip + ci
    s_len = x.shape[1]

    shards = [w_in[0].T, w_branch_hgrn[0], w_branch_attn[0], w_out[0], w_ffn_gate[0].T, w_ffn_up[0].T,
              w_ffn_down[0]]
    bufs = _cast_place(shards, ci, dev)

    lbrow = jnp.pad(hgrn_lb_logits.reshape(1, 512), ((0, 0), (0, D - 512)))
    blk = jnp.concatenate([c, lbrow, jnp.zeros((6, D), F32)], axis=0)
    nada = w_ada.shape[2]
    b_sh = lax.dynamic_slice(b_ada, (0, chip * nada), (1, nada))
    g0, c16, g1, in8 = _prologue(blk, c_ctx[None], w_ada[0], b_sh, bufs[0])
    lg = g0[0::2, 1, :512].reshape(4, 2, 2, 128).transpose(1, 2, 0, 3).reshape(2, 2, HGW)
    modall = g1[0::2].transpose(1, 0, 2).reshape(16, 4 * nada)
    mod = lax.dynamic_slice(modall, (dev, 0), (1, 6 * D)).reshape(6, D)
    modc = modall[8].reshape(6, D)[:2]

    sq, gx, _, small, rs = _local_step(
        x[0], ctx[0], loss_target[0], mod, modc, norm_mix_w, norm_ffn_w, lg, hgrn_norm_w, q_norm_w,
        k_norm_w, attn_sinks[0], in8.reshape(NCOL, D), bufs[1:], dist=(ci, chip))

    def whole(r):
        return r.reshape(2 * r.shape[1], r.shape[2])

    g_dn, g_g, g_u = [whole(r) for r in rs["ffn_done"]]
    g_bh, g_ba, g_o = [whole(r) for r in rs["mix_done"]]
    in_pairs = rs["in_pairs"]

    g2, tot = _ag_small(small["raw"], sq)
    loss = 0.5 * jnp.sum(tot[24]) / D
    dmodc_tot = jnp.pad(tot[6:8].reshape(1, 2 * D), ((0, 0), (0, 4 * D)))
    g_b_ada = tot[0:6].reshape(1, 6 * D) + dmodc_tot
    dmod16 = jnp.concatenate([g2[:, 0:6].reshape(8, 6 * D), dmodc_tot, jnp.zeros((7, 6 * D), F32)], axis=0)
    (g_w_ada, gc_part), in_contribs = _ada_bwd(
        c16, lax.dynamic_slice(dmod16, (0, chip * nada), (16, nada)), w_ada[0],
        carry=_carry_chipx(in_pairs, rows=rs["rest_rows"], into=rs["in_part"]))
    g3, = _allgather([gc_part[8:16]], name="ag_cctx")
    g_c_ctx = _cctx_grad(g3[0::2, 0], c_ctx[None])[0]
    g_nw1 = tot[8:9]
    g_nw2 = tot[9:10]
    g_hw = tot[10, :HGW].reshape(4, HGD).sum(0, keepdims=True)
    g_qnw = tot[10, HGW:].reshape(8, HDIM).sum(0, keepdims=True)
    g_knw = tot[11, :128].reshape(2, HDIM).sum(0, keepdims=True)
    g_sinks = tot[16:24, 0][None]
    g_lg = lax.dynamic_slice(tot[12:16, :HGW].reshape(2, 2, HGW), (0, 0, chip * 128), (2, 2, 128))

    names = ["c_ctx", "w_ada", "b_ada", "norm_mix_w", "norm_ffn_w", "w_in", "hgrn_lb_logits", "hgrn_norm_w",
             "q_norm_w", "k_norm_w", "attn_sinks", "w_branch_hgrn", "w_branch_attn", "w_out", "w_ffn_gate",
             "w_ffn_up", "w_ffn_down"]
    ws = dict(zip(names, [c_ctx, w_ada, b_ada, norm_mix_w, norm_ffn_w, w_in, hgrn_lb_logits, hgrn_norm_w,
                          q_norm_w, k_norm_w, attn_sinks, w_branch_hgrn, w_branch_attn, w_out, w_ffn_gate,
                          w_ffn_up, w_ffn_down]))
    ms = dict(zip(names, [m_c_ctx, m_w_ada, m_b_ada, m_norm_mix_w, m_norm_ffn_w, m_w_in, m_hgrn_lb_logits,
                          m_hgrn_norm_w, m_q_norm_w, m_k_norm_w, m_attn_sinks, m_w_branch_hgrn,
                          m_w_branch_attn, m_w_out, m_w_ffn_gate, m_w_ffn_up, m_w_ffn_down]))
    vs = dict(zip(names, [v_c_ctx, v_w_ada, v_b_ada, v_norm_mix_w, v_norm_ffn_w, v_w_in, v_hgrn_lb_logits,
                          v_hgrn_norm_w, v_q_norm_w, v_k_norm_w, v_attn_sinks, v_w_branch_hgrn,
                          v_w_branch_attn, v_w_out, v_w_ffn_gate, v_w_ffn_up, v_w_ffn_down]))
    transposed = ("w_in", "w_ffn_gate", "w_ffn_up")

    def view(a, n):
        return a[0].T if n in transposed else a[0]

    def unview(a, n):
        return a.T[None] if n in transposed else a[None]

    delta, new_m, new_v, grads = {}, {}, {}, {}

    def big_adamw(group, gs, name, carry=None):
        (d_, m_, v_), extra = _adamw_multi([view(ws[n], n) for n in group], gs, [view(ms[n], n) for n in group],
                                           [view(vs[n], n) for n in group], name=name, carry=carry)
        for i, n in enumerate(group):
            grads[n], delta[n], new_m[n], new_v[n] = (unview(gs[i], n), unview(d_[i], n), unview(m_[i], n),
                                                      unview(v_[i], n))
        return extra

    big_adamw(["w_ffn_down", "w_ffn_gate", "w_ffn_up", "w_out", "w_branch_hgrn", "w_branch_attn"],
              [g_dn, g_g, g_u, g_o, g_bh, g_ba], "adamw_first")
    in_reds = _rs_chip_add(in_pairs, in_contribs, ci, chip)
    g_in, = [whole(r) for r in _rs_sibling_gather(in_reds)]
    big_adamw(["w_in", "w_ada"], [g_in, g_w_ada], "adamw_second")
    grads.update(c_ctx=g_c_ctx, b_ada=g_b_ada, norm_mix_w=g_nw1, norm_ffn_w=g_nw2, hgrn_lb_logits=g_lg,
                 hgrn_norm_w=g_hw, q_norm_w=g_qnw, k_norm_w=g_knw, attn_sinks=g_sinks)
    small_names = [n for n in names if n not in delta]

    def two_d(a):
        return a.reshape(1, -1) if a.ndim == 1 else a

    sd, sm_, sv = _adamw_small(*[[two_d(d[n]) for n in small_names] for d in (ws, grads, ms, vs)])
    for i, n in enumerate(small_names):
        for dst, src in ((delta, sd), (new_m, sm_), (new_v, sv)):
            dst[n] = src[i].reshape(ws[n].shape)
    return (loss, gx[None], *[grads[n] for n in names], *[delta[n] for n in names],
            *[new_m[n] for n in names], *[new_v[n] for n in names])
```

```python
import functools

import numpy as np
import jax
import jax.numpy as jnp
from jax import lax
from jax.experimental import pallas as pl
from jax.experimental.pallas import tpu as pltpu

F32 = jnp.float32
BF16 = jnp.bfloat16
HI = lax.Precision.HIGHEST
MESH = pl.DeviceIdType.MESH

D = 1024
L = 256
TM = 256
HGW = 512
HGD = 128
CH = 32
ATW = 512
HDIM = 64
BLK = 128
GRID_W = 64
DFF = 2816
NCOL = 5376
EPS = 1e-6
ROPE_THETA = 10000.0
BF16_SUBLANES = 16

C_FB, C_INP, C_QHG, C_FF = 0, 1, 2, 3
C_GATES = 1
C_GHG, C_QRAW = 8, 9
C_KV = 20
C_QKV = 6

ADAM_LR, ADAM_B1, ADAM_B2, ADAM_EPS, ADAM_WD, ADAM_STEP = 0.001, 0.9, 0.999, 1e-08, 0.01, 10

NN = (((1,), (0,)), ((), ()))
NT = (((1,), (1,)), ((), ()))
TN = (((0,), (0,)), ((), ()))


def _dot(a, b, dims=NN, prec=None):
    return lax.dot_general(a, b, dims, precision=prec, preferred_element_type=F32)


def _bdot(a, b, dims=NN):
    return _dot(a.astype(BF16), b.astype(BF16), dims)


def _sig(x):
    return 1.0 / (1.0 + jnp.exp(-x))


class _Carry:
    def __init__(self, ins, outs, aliases, scratch, phases, peers):
        self.ins, self.outs, self.aliases, self.scratch, self.phases = ins, outs, aliases, scratch, phases
        self.peers = peers


BARRIER_IDS = {"sib": 1, "chips": 2, "both": 3}


def _peer_barrier(kind):
    x, y, c = _place()
    peers = []
    if kind in ("sib", "both"):
        peers.append((x, y, 1 - c))
    if kind in ("chips", "both"):
        peers += [(1 - x, y, c), (x, 1 - y, c), (1 - x, 1 - y, c)]
    bar = pltpu.get_barrier_semaphore()
    for peer in peers:
        pl.semaphore_signal(bar, inc=1, device_id=peer, device_id_type=MESH)
    pl.semaphore_wait(bar, len(peers))


def _in_hbm(args):
    return [pltpu.with_memory_space_constraint(a, pltpu.HBM) for a in args]


def _out_hbm(shapes):
    if isinstance(shapes, (list, tuple)):
        return [pltpu.HBM(s.shape, s.dtype) for s in shapes]
    return pltpu.HBM(shapes.shape, shapes.dtype)


def _carry_join(a, b):
    na_in, na_out, na_sc = len(a.ins), len(a.outs), len(a.scratch)
    aliases = dict(a.aliases)
    aliases.update({na_in + i: na_out + o for i, o in b.aliases.items()})

    def phases(ins, outs, sems):
        pa = a.phases(ins[:na_in], outs[:na_out], sems[:na_sc])
        pb = b.phases(ins[na_in:], outs[na_out:], sems[na_sc:])

        def both(fa, fb):
            if fa is None and fb is None:
                return None

            def run():
                for fn in (fa, fb):
                    if fn is not None:
                        fn()
            return run

        return tuple(both(fa, fb) for fa, fb in zip(pa, pb))

    return _Carry(list(a.ins) + list(b.ins), list(a.outs) + list(b.outs), aliases,
                  list(a.scratch) + list(b.scratch), phases, a.peers if a.peers == b.peers else "both")


def _pcall(body, *, name, grid, in_specs, out_specs, out_shape, scratch=(), aliases=None, vmem_mb=48,
           carry=None):
    params = pltpu.CompilerParams(dimension_semantics=("arbitrary",) * len(grid),
                                  vmem_limit_bytes=vmem_mb << 20)
    if carry is None:
        plain = pl.pallas_call(
            body, name=name, grid=grid, in_specs=in_specs, out_specs=out_specs, out_shape=_out_hbm(out_shape),
            scratch_shapes=list(scratch), input_output_aliases=aliases or {}, compiler_params=params)
        return lambda *args: plain(*_in_hbm(args))
    single = not isinstance(out_shape, (list, tuple))
    out_specs_l = [out_specs] if single else list(out_specs)
    out_shape_l = [out_shape] if single else list(out_shape)
    n_in, n_out, n_sc = len(in_specs), len(out_shape_l), len(scratch)
    k_in, k_out = len(carry.ins), len(carry.outs)
    nsteps = int(np.prod(grid))
    assert nsteps >= 3

    def wrapped(*refs):
        ins, cins = refs[:n_in], refs[n_in:n_in + k_in]
        o0 = n_in + k_in
        outs, couts = refs[o0:o0 + n_out], refs[o0 + n_out:o0 + n_out + k_out]
        s0 = o0 + n_out + k_out
        sc, csc = refs[s0:s0 + n_sc], refs[s0 + n_sc:]
        step = pl.program_id(0)
        for ax in range(1, len(grid)):
            step = step * grid[ax] + pl.program_id(ax)
        start, mid, end = carry.phases(cins, couts, csc)

        @pl.when(step == 0)
        def _():
            _peer_barrier(carry.peers)
            start()

        body(*ins, *outs, *sc)
        if mid is not None:
            pl.when(step == nsteps - 2)(mid)
        pl.when(step == nsteps - 1)(end)

    all_aliases = dict(aliases or {})
    all_aliases.update({n_in + i: n_out + o for i, o in carry.aliases.items()})
    call = pl.pallas_call(
        wrapped, name=name, grid=grid, in_specs=list(in_specs) + [ANY] * k_in,
        out_specs=out_specs_l + [ANY] * k_out, out_shape=_out_hbm(out_shape_l + list(carry.outs)),
        scratch_shapes=list(scratch) + list(carry.scratch), input_output_aliases=all_aliases,
        compiler_params=pltpu.CompilerParams(dimension_semantics=("arbitrary",) * len(grid),
                                             vmem_limit_bytes=vmem_mb << 20,
                                             collective_id=BARRIER_IDS[carry.peers]))

    def run(*args):
        res = call(*_in_hbm(args), *carry.ins)
        core = res[:n_out]
        return (core[0] if single else list(core)), list(res[n_out:])

    return run


def _full(shape):
    nd = len(shape)
    return pl.BlockSpec(shape, lambda *_: (0,) * nd)


ANY = pl.BlockSpec(memory_space=pl.ANY)


NT_IN = NCOL // 256


def _src_block(j):
    return j + jnp.where(j < 4, 2, jnp.where(j < 6, 3, jnp.where(j < 8, -6, jnp.where(
        j < 16, 5, jnp.where(j < 20, -7, -14)))))


def _mm_in(h, wt, tm, carry=None):
    tt = h.shape[0]

    def body(h_ref, w_ref, o_ref):
        o_ref[...] = _bdot(h_ref[...], w_ref[...], NT)

    return _pcall(body, name="mm_in", grid=(tt // tm, NT_IN),
                  in_specs=[pl.BlockSpec((tm, D), lambda i, j: (i, 0)),
                            pl.BlockSpec((256, D), lambda i, j: (_src_block(j), 0))],
                  out_specs=pl.BlockSpec((tm, 256), lambda i, j: (i, j)),
                  out_shape=jax.ShapeDtypeStruct((tt, NCOL), F32), carry=carry)(h, wt)


def _mm_dh(dp, wt, tm, carry=None):
    tt = dp.shape[0]
    per, ng = 3, NT_IN // 3

    def body(d_ref, w0, w1, w2, o_ref, acc):
        kk = pl.program_id(1)

        @pl.when(kk == 0)
        def _():
            acc[...] = jnp.zeros_like(acc)

        acc[...] += (_bdot(d_ref[:, 0:256], w0[...]) + _bdot(d_ref[:, 256:512], w1[...])
                     + _bdot(d_ref[:, 512:768], w2[...]))

        @pl.when(kk == ng - 1)
        def _():
            o_ref[...] = acc[...]

    wspecs = [pl.BlockSpec((256, D), functools.partial(lambda t, i, kk: (_src_block(per * kk + t), 0), t))
              for t in range(per)]
    return _pcall(body, name="mm_dh", grid=(tt // tm, ng),
                  in_specs=[pl.BlockSpec((tm, per * 256), lambda i, kk: (i, kk))] + wspecs,
                  out_specs=pl.BlockSpec((tm, D), lambda i, kk: (i, 0)),
                  out_shape=jax.ShapeDtypeStruct((tt, D), F32), scratch=[pltpu.VMEM((tm, D), F32)],
                  carry=carry)(dp, wt, wt, wt)


def _mm_gin(dp, h, tk):
    tt = dp.shape[0]
    nk = tt // tk

    def body(d_ref, h_ref, o_ref, acc):
        kk = pl.program_id(1)

        @pl.when(kk == 0)
        def _():
            acc[...] = jnp.zeros_like(acc)

        acc[...] += _bdot(d_ref[...], h_ref[...], TN)

        @pl.when(kk == nk - 1)
        def _():
            o_ref[...] = acc[...].astype(BF16)

    return _pcall(body, name="mm_gin", grid=(NT_IN, nk),
                  in_specs=[pl.BlockSpec((tk, 256), lambda j, kk: (kk, j)),
                            pl.BlockSpec((tk, D), lambda j, kk: (kk, 0))],
                  out_specs=pl.BlockSpec((256, D), lambda j, kk: (_src_block(j), 0)),
                  out_shape=jax.ShapeDtypeStruct((NCOL, D), BF16), scratch=[pltpu.VMEM((256, D), F32)])(dp, h)


def _tok_specs():
    assert L == TM
    return [_full((TM, D)), pl.BlockSpec((TM, D), lambda i: (jnp.maximum(i - 1, 0), 0))]


def _mod1(ctx, x, nw, ss):
    rows = L + x.shape[0]

    def body(c_ref, x_ref, nw_ref, ss_ref, h_ref):
        t = jnp.where(pl.program_id(0) == 0, c_ref[...], x_ref[...])
        r = lax.rsqrt(jnp.mean(t * t, axis=-1, keepdims=True) + EPS)
        s = ss_ref[0]
        h_ref[...] = ((t * r * nw_ref[...]) * (1.0 + s[1:2]) + s[0:1]).astype(BF16)

    return _pcall(body, name="mod1", grid=(rows // TM,),
                  in_specs=_tok_specs() + [_full((1, D)),
                                           pl.BlockSpec((1, 2, D), lambda i: (jnp.minimum(i, 1), 0, 0))],
                  out_specs=pl.BlockSpec((TM, D), lambda i: (i, 0)),
                  out_shape=jax.ShapeDtypeStruct((rows, D), BF16))(ctx, x, nw, ss)


def _norm_bwd_rows(x, dh, nw, scale):
    r = lax.rsqrt(jnp.mean(x * x, axis=-1, keepdims=True) + EPS)
    xh = x * r
    dxh = dh * ((1.0 + scale) * nw)
    dx = r * (dxh - xh * jnp.mean(dxh * xh, axis=-1, keepdims=True))
    return dx, xh


def _out_proj_mod2(mixed, w_o, x, g1, nw2, ss2):
    s_len = x.shape[0]
    tm = 512

    def body(m_ref, w_ref, x_ref, g_ref, nw_ref, ss_ref, ao_ref, x1_ref, h_ref):
        ao = _bdot(m_ref[...], w_ref[...])
        ao_ref[...] = ao.astype(BF16)
        x1 = x_ref[...] + g_ref[...] * ao
        x1_ref[...] = x1
        r = lax.rsqrt(jnp.mean(x1 * x1, axis=-1, keepdims=True) + EPS)
        s = ss_ref[0]
        h_ref[...] = ((x1 * r * nw_ref[...]) * (1.0 + s[1:2]) + s[0:1]).astype(BF16)

    row = pl.BlockSpec((tm, D), lambda i: (i, 0))
    f = jax.ShapeDtypeStruct((s_len, D), F32)
    return _pcall(body, name="out_proj_mod2", grid=(s_len // tm,),
                  in_specs=[row, _full((D, D)), row, _full((1, D)), _full((1, D)), _full((1, 2, D))],
                  out_specs=[row, row, row],
                  out_shape=[jax.ShapeDtypeStruct((s_len, D), BF16), f,
                             jax.ShapeDtypeStruct((s_len, D), BF16)])(mixed, w_o, x, g1, nw2, ss2)


TS = 1024


def _acc_call(body, *, name, grid, in_specs, out_specs, out_shape, acc_shapes, args, carry=None):
    return _pcall(body, name=name, grid=grid, in_specs=in_specs, out_specs=out_specs, out_shape=out_shape,
                  scratch=[pltpu.VMEM(s, F32) for s in acc_shapes], carry=carry)(*args)


def _ffn_up(h2, g4, u4, carry=None):
    s_len = h2.shape[0]
    ns = g4.shape[1]

    def body(h_ref, g_ref, u_ref, a_ref, b_ref, z_ref):
        h = h_ref[...]
        a = _bdot(h, g_ref[0], NT)
        b = _bdot(h, u_ref[0], NT)
        a_ref[0] = a.astype(BF16)
        b_ref[0] = b.astype(BF16)
        z_ref[0] = (a * _sig(a) * b).astype(BF16)

    w = pl.BlockSpec((1, ns, D), lambda i, j: (j, 0, 0))
    o = pl.BlockSpec((1, TS, ns), lambda i, j: (j, i, 0))
    f = jax.ShapeDtypeStruct((4, s_len, ns), BF16)
    return _pcall(body, name="ffn_up", grid=(s_len // TS, 4),
                  in_specs=[pl.BlockSpec((TS, D), lambda i, j: (i, 0)), w, w], out_specs=[o, o, o],
                  out_shape=[f, f, jax.ShapeDtypeStruct((4, s_len, ns), BF16)], carry=carry)(h2, g4, u4)


def _ffn_down_loss(z4, dn4, x1, g2, tgt):
    _, s_len, ns = z4.shape

    def body(z_ref, w_ref, x1_ref, g_ref, t_ref, sq_ref, dx2_ref, dyb_ref, dg_ref, acc):
        i, j = pl.program_id(0), pl.program_id(1)

        @pl.when((i == 0) & (j == 0))
        def _():
            sq_ref[...] = jnp.zeros_like(sq_ref)
            dg_ref[...] = jnp.zeros_like(dg_ref)

        @pl.when(j == 0)
        def _():
            acc[...] = jnp.zeros_like(acc)

        acc[...] += _bdot(z_ref[0], w_ref[0])

        @pl.when(j == 3)
        def _():
            y_ = acc[...]
            g = g_ref[...]
            e = x1_ref[...] + g * y_ - t_ref[...]
            sq_ref[...] += jnp.sum(e * e, axis=0, keepdims=True)
            dx2 = e * (1.0 / D)
            dx2_ref[...] = dx2
            dyb_ref[...] = (g * dx2).astype(BF16)
            dg_ref[...] += jnp.sum(dx2 * y_, axis=0, keepdims=True)

    row = pl.BlockSpec((TS, D), lambda i, j: (i, 0))
    vec = _full((1, D))
    return _acc_call(body, name="ffn_down_loss", grid=(s_len // TS, 4),
                     in_specs=[pl.BlockSpec((1, TS, ns), lambda i, j: (j, i, 0)),
                               pl.BlockSpec((1, ns, D), lambda i, j: (j, 0, 0)), row, vec, row],
                     out_specs=[vec, row, row, vec],
                     out_shape=[jax.ShapeDtypeStruct((1, D), F32), jax.ShapeDtypeStruct((s_len, D), F32),
                                jax.ShapeDtypeStruct((s_len, D), BF16), jax.ShapeDtypeStruct((1, D), F32)],
                     acc_shapes=[(TS, D)], args=(z4, dn4, x1, g2, tgt))


def _ffn_dz(dyb, dn4, a4, b4):
    _, s_len, ns = a4.shape

    def body(dy_ref, w_ref, a_ref, b_ref, da_ref, db_ref):
        dz = _bdot(dy_ref[...], w_ref[0], NT)
        a = a_ref[0].astype(F32)
        s = _sig(a)
        da_ref[0] = (dz * b_ref[0].astype(F32) * (s * (1.0 + a * (1.0 - s)))).astype(BF16)
        db_ref[0] = (dz * (a * s)).astype(BF16)

    t = pl.BlockSpec((1, TS, ns), lambda i, j: (j, i, 0))
    o = jax.ShapeDtypeStruct((4, s_len, ns), BF16)
    return _pcall(body, name="ffn_dz", grid=(s_len // TS, 4),
                  in_specs=[pl.BlockSpec((TS, D), lambda i, j: (i, 0)),
                            pl.BlockSpec((1, ns, D), lambda i, j: (j, 0, 0)), t, t],
                  out_specs=[t, t], out_shape=[o, o])(dyb, dn4, a4, b4)


def _ffn_gdn(z4, dyb):
    _, s_len, ns = z4.shape
    tk = min(s_len, 2 * TS)
    nk = s_len // tk

    def body(z_ref, dy_ref, o_ref, acc):
        t = pl.program_id(1)

        @pl.when(t == 0)
        def _():
            acc[...] = jnp.zeros_like(acc)

        acc[...] += _bdot(z_ref[0], dy_ref[...], TN)

        @pl.when(t == nk - 1)
        def _():
            o_ref[0] = acc[...].astype(o_ref.dtype)

    return _acc_call(body, name="ffn_gdn", grid=(4, nk),
                     in_specs=[pl.BlockSpec((1, tk, ns), lambda j, t: (j, t, 0)),
                               pl.BlockSpec((tk, D), lambda j, t: (t, 0))],
                     out_specs=pl.BlockSpec((1, ns, D), lambda j, t: (j, 0, 0)),
                     out_shape=jax.ShapeDtypeStruct((4, ns, D), BF16), acc_shapes=[(ns, D)], args=(z4, dyb))


def _ffn_dh2(da4, db4, g4, u4, carry=None):
    _, s_len, ns = da4.shape

    def body(da_ref, db_ref, g_ref, u_ref, o_ref, acc):
        j = pl.program_id(1)

        @pl.when(j == 0)
        def _():
            acc[...] = jnp.zeros_like(acc)

        acc[...] += _bdot(da_ref[0], g_ref[0]) + _bdot(db_ref[0], u_ref[0])

        @pl.when(j == 3)
        def _():
            o_ref[...] = acc[...]

    t = pl.BlockSpec((1, TS, ns), lambda i, j: (j, i, 0))
    w = pl.BlockSpec((1, ns, D), lambda i, j: (j, 0, 0))
    return _acc_call(body, name="ffn_dh2", grid=(s_len // TS, 4), in_specs=[t, t, w, w],
                     out_specs=pl.BlockSpec((TS, D), lambda i, j: (i, 0)),
                     out_shape=jax.ShapeDtypeStruct((s_len, D), F32), acc_shapes=[(TS, D)],
                     args=(da4, db4, g4, u4), carry=carry)


def _ffn_ggu(h2, da4, db4, carry=None):
    _, s_len, ns = da4.shape
    nk = s_len // TS

    def body(h_ref, da_ref, db_ref, gg_ref, gu_ref, acc_g, acc_u):
        t = pl.program_id(1)

        @pl.when(t == 0)
        def _():
            acc_g[...] = jnp.zeros_like(acc_g)
            acc_u[...] = jnp.zeros_like(acc_u)

        h = h_ref[...]
        acc_g[...] += _bdot(da_ref[0], h, TN)
        acc_u[...] += _bdot(db_ref[0], h, TN)

        @pl.when(t == nk - 1)
        def _():
            gg_ref[0] = acc_g[...].astype(BF16)
            gu_ref[0] = acc_u[...].astype(BF16)

    d = pl.BlockSpec((1, TS, ns), lambda j, t: (j, t, 0))
    o = pl.BlockSpec((1, ns, D), lambda j, t: (j, 0, 0))
    f = jax.ShapeDtypeStruct((4, ns, D), BF16)
    return _acc_call(body, name="ffn_ggu", grid=(4, nk),
                     in_specs=[pl.BlockSpec((TS, D), lambda j, t: (t, 0)), d, d], out_specs=[o, o],
                     out_shape=[f, f], acc_shapes=[(ns, D), (ns, D)], args=(h2, da4, db4), carry=carry)


def _mod2_bwd(x1, dh2, dx2, ao, nw2, ss2, g1):
    s_len = x1.shape[0]

    def body(x1_ref, dh_ref, dx2_ref, ao_ref, nw_ref, ss_ref, g_ref,
             dx1_ref, da_ref, dss_ref, dnw_ref, dg_ref):
        i = pl.program_id(0)

        @pl.when(i == 0)
        def _():
            dss_ref[...] = jnp.zeros_like(dss_ref)
            dnw_ref[...] = jnp.zeros_like(dnw_ref)
            dg_ref[...] = jnp.zeros_like(dg_ref)

        dh = dh_ref[...]
        nw = nw_ref[...]
        scale = ss_ref[0][1:2]
        dxn, xh = _norm_bwd_rows(x1_ref[...], dh, nw, scale)
        dx1 = dx2_ref[...] + dxn
        dx1_ref[...] = dx1
        da_ref[...] = (g_ref[...] * dx1).astype(BF16)
        dg_ref[...] += jnp.sum(dx1 * ao_ref[...].astype(F32), axis=0, keepdims=True)
        dsh = jnp.sum(dh, axis=0, keepdims=True)
        dsc = jnp.sum(dh * xh * nw, axis=0, keepdims=True)
        dss_ref[...] += jnp.concatenate([dsh, dsc], axis=0)
        dnw_ref[...] += jnp.sum(dh * xh * (1.0 + scale), axis=0, keepdims=True)

    row = pl.BlockSpec((TM, D), lambda i: (i, 0))
    vec = _full((1, D))
    return _pcall(body, name="mod2_bwd", grid=(s_len // TM,),
                  in_specs=[row, row, row, row, vec, _full((1, 2, D)), vec],
                  out_specs=[row, row, _full((2, D)), vec, vec],
                  out_shape=[jax.ShapeDtypeStruct((s_len, D), F32), jax.ShapeDtypeStruct((s_len, D), BF16),
                             jax.ShapeDtypeStruct((2, D), F32), jax.ShapeDtypeStruct((1, D), F32),
                             jax.ShapeDtypeStruct((1, D), F32)])(x1, dh2, dx2, ao, nw2, ss2, g1)


def _mod1_bwd(ctx, x, dh, dx1, nw1, ss1):
    s_len = dx1.shape[0]
    tt = L + s_len

    def body(c_ref, x_ref, dh_ref, dx1_ref, nw_ref, ss_ref, dx_ref, dss_ref, dnw_ref):
        i = pl.program_id(0)
        tok = jnp.where(i == 0, c_ref[...], x_ref[...])

        @pl.when(i == 0)
        def _():
            dnw_ref[...] = jnp.zeros_like(dnw_ref)

        @pl.when(i <= 1)
        def _():
            dss_ref[...] = jnp.zeros_like(dss_ref)

        dh_ = dh_ref[...]
        nw = nw_ref[...]
        scale = ss_ref[0][1:2]
        dxn, xh = _norm_bwd_rows(tok, dh_, nw, scale)

        @pl.when(i >= 1)
        def _():
            dx_ref[...] = dx1_ref[...] + dxn

        dsh = jnp.sum(dh_, axis=0, keepdims=True)
        dsc = jnp.sum(dh_ * xh * nw, axis=0, keepdims=True)
        dss_ref[...] += jnp.concatenate([dsh, dsc], axis=0)[None]
        dnw_ref[...] += jnp.sum(dh_ * xh * (1.0 + scale), axis=0, keepdims=True)

    row = pl.BlockSpec((TM, D), lambda i: (i, 0))
    lat = pl.BlockSpec((TM, D), lambda i: (jnp.maximum(i - 1, 0), 0))
    sel = pl.BlockSpec((1, 2, D), lambda i: (jnp.minimum(i, 1), 0, 0))
    return _pcall(body, name="mod1_bwd", grid=(tt // TM,),
                  in_specs=_tok_specs() + [row, lat, _full((1, D)), sel],
                  out_specs=[lat, sel, _full((1, D))],
                  out_shape=[jax.ShapeDtypeStruct((s_len, D), F32), jax.ShapeDtypeStruct((2, 2, D), F32),
                             jax.ShapeDtypeStruct((1, D), F32)])(ctx, x, dh, dx1, nw1, ss1)


def _rows(c):
    return slice(c * CH, (c + 1) * CH)


def _chunk_masks(rev, transpose=False):
    r = lax.broadcasted_iota(jnp.int32, (TM, TM), 0)
    c = lax.broadcasted_iota(jnp.int32, (TM, TM), 1)
    same = (r // CH) == (c // CH)
    before = (c >= r) if (rev != transpose) else (c <= r)
    return same & before, same


def _chunk_scan(x, rev, transpose=False):
    r = lax.broadcasted_iota(jnp.int32, (CH, CH), 0)
    c = lax.broadcasted_iota(jnp.int32, (CH, CH), 1)
    tri = ((c >= r) if (rev != transpose) else (c <= r)).astype(F32)
    return jnp.concatenate([_dot(tri, x[_rows(ch)], prec=HI) for ch in range(x.shape[0] // CH)], axis=0)


def _chunk_total(x):
    return jnp.concatenate([jnp.broadcast_to(jnp.sum(x[_rows(ch)], axis=0, keepdims=True), (CH, x.shape[1]))
                            for ch in range(x.shape[0] // CH)], axis=0)


def _hgrn_gate(fl, qraw, lg):
    lb = 1.0 / (1.0 + jnp.exp(lg[1:2] - lg[0:1]))
    sg = _sig(fl)
    f = lb + (1.0 - lb) * sg
    q = qraw * _sig(qraw) * (HGD ** -0.5)
    return lb, sg, f, q


def _hgrn_fwd(p, lg, *, rev, carry=None, readout=None):
    tt = p.shape[0]
    nt = tt // TM
    ncht = TM // CH
    d = 1 if rev else 0

    def tile_of(s):
        return jnp.where(s == 0, 0, nt - s) if rev else s

    def body(*refs):
        if readout is None:
            f_ref, inp_ref, q_ref, lg_ref, o_ref, st_ref, state = refs
        else:
            f_ref, inp_ref, q_ref, lg_ref, oo_ref, g_ref, hw_ref, o_ref, st_ref, y_ref, state = refs
        s = pl.program_id(0)

        @pl.when(s == 0)
        def _():
            state[...] = jnp.zeros_like(state)

        _, _, f, q = _hgrn_gate(f_ref[...], q_ref[...], lg_ref[0])
        lf = jnp.log(f)
        causal, _ = _chunk_masks(rev)
        cum = _chunk_scan(lf, rev)
        tot = _chunk_total(lf)
        qd = (q * jnp.exp(cum)).astype(BF16)
        kd = ((1.0 - f) * jnp.exp(-cum)).astype(BF16)
        ke = ((1.0 - f) * jnp.exp(tot - cum)).astype(BF16)
        et = jnp.exp(tot)
        v = inp_ref[...].astype(BF16)
        order = range(ncht - 1, -1, -1) if rev else range(ncht)
        outs = []
        for h in range(4):
            sl = slice(h * HGD, (h + 1) * HGD)
            qd_, kd_, ke_, v_ = qd[:, sl], kd[:, sl], ke[:, sl], v[:, sl]
            pm = jnp.where(causal, _dot(qd_, kd_, NT), 0.0).astype(BF16)
            o_h = _dot(pm, v_)
            upd = [_dot(v_[_rows(c)], ke_[_rows(c)], TN) for c in range(ncht)]
            st = state[h]
            for c in order:
                st_ref[c, h] = st
                st = st * et[c * CH:c * CH + 1, sl] + upd[c]
            state[h] = st
            inter = [_dot(qd_[_rows(c)], st_ref[c, h].astype(BF16), NT) for c in range(ncht)]
            outs.append(o_h + jnp.concatenate(inter, axis=0))
        o_tile = jnp.concatenate(outs, axis=1)
        o_ref[...] = o_tile
        if readout is not None:
            @pl.when(tile_of(s) >= 1)
            def _():
                g = g_ref[...]
                y_ref[...] = (_head_rms(oo_ref[...] + o_tile, None, 4) * hw_ref[...] * (g * _sig(g))).astype(BF16)

    def col(cb):
        return pl.BlockSpec((TM, HGW), lambda s: (tile_of(s), cb))

    in_specs = [col(C_FB if rev else C_FF), col(C_INP), col(C_QHG), pl.BlockSpec((1, 2, HGW), lambda s: (d, 0, 0))]
    out_specs = [col(0), pl.BlockSpec((ncht, 4, HGD, HGD), lambda s: (tile_of(s), 0, 0, 0))]
    out_shape = [jax.ShapeDtypeStruct((tt, HGW), F32), jax.ShapeDtypeStruct((nt * ncht, 4, HGD, HGD), F32)]
    args = [p, p, p, lg]
    if readout is not None:
        in_specs += [col(0), col(C_GHG), _full((1, HGW))]
        args += [readout[0], p, readout[1]]
        assert rev
        out_specs.append(pl.BlockSpec((TM, HGW), lambda s: (jnp.where(s == 0, nt - 2, tile_of(s) - 1), 0)))
        out_shape.append(jax.ShapeDtypeStruct((tt - L, HGW), BF16))
    return _pcall(body, name="hgrn_fwd_rev" if rev else "hgrn_fwd", grid=(nt,), in_specs=in_specs,
                  out_specs=out_specs, out_shape=out_shape, scratch=[pltpu.VMEM((4, HGD, HGD), F32)],
                  carry=carry)(*args)


def _hgrn_bwd(p, lg, do, st, dp, prev, *, rev, carry=None):
    tt = p.shape[0]
    nt = tt // TM
    ncht = TM // CH
    d = 1 if rev else 0
    second = prev is not None

    def tile_of(s):
        return jnp.where(s == nt - 1, 0, s + 1) if rev else nt - 1 - s

    def body(*refs):
        if second:
            (f_ref, inp_ref, q_ref, lg_ref, do_ref, st_ref, dvp_ref, dqp_ref, _dp_in,
             dp_ref, dlg_ref, dstate) = refs
        else:
            (f_ref, inp_ref, q_ref, lg_ref, do_ref, st_ref, _dp_in,
             dp_ref, dv_ref, dq_ref, dlg_ref, dstate) = refs
        s = pl.program_id(0)
        tile = tile_of(s)

        @pl.when(s == 0)
        def _():
            dstate[...] = jnp.zeros_like(dstate)
            dlg_ref[...] = jnp.zeros_like(dlg_ref)

        qraw = q_ref[...]
        lb, sg, f, q = _hgrn_gate(f_ref[...], qraw, lg_ref[0])
        lf = jnp.log(f)
        causal, _ = _chunk_masks(rev)
        causal_t, _ = _chunk_masks(rev, transpose=True)
        cum = _chunk_scan(lf, rev)
        tot = _chunk_total(lf)
        ea, eb, ee, et = jnp.exp(cum), jnp.exp(-cum), jnp.exp(tot - cum), jnp.exp(tot)
        qdf, kdf, kef = q * ea, (1.0 - f) * eb, (1.0 - f) * ee
        qd, kd, ke = qdf.astype(BF16), kdf.astype(BF16), kef.astype(BF16)
        v = inp_ref[...].astype(BF16)
        dob = jnp.where(tile == 0, 0.0, do_ref[...]).astype(BF16)
        order = range(ncht) if rev else range(ncht - 1, -1, -1)
        dq_l, dk_l, dv_l, dcum_l, dtot_l = [], [], [], [], []
        for h in range(4):
            sl = slice(h * HGD, (h + 1) * HGD)
            qd_, kd_, ke_, v_, do_ = qd[:, sl], kd[:, sl], ke[:, sl], v[:, sl], dob[:, sl]
            pmt = jnp.where(causal_t, _dot(kd_, qd_, NT), 0.0).astype(BF16)
            dpm = jnp.where(causal, _dot(do_, v_, NT), 0.0).astype(BF16)
            dpmt = jnp.where(causal_t, _dot(v_, do_, NT), 0.0).astype(BF16)
            dv = _dot(pmt, do_)
            dqd = _dot(dpm, kd_)
            dkd = _dot(dpmt, qd_)
            upd = [_dot(do_[_rows(c)], qd_[_rows(c)], TN) for c in range(ncht)]
            ds = dstate[h]
            ds1 = [None] * ncht
            for c in order:
                ds1[c] = ds
                ds = ds * et[c * CH:c * CH + 1, sl] + upd[c]
            dstate[h] = ds
            dke_c, dv_c, dqd_c, dtot_c = [], [], [], []
            for c in range(ncht):
                st0 = st_ref[c, h]
                dsb = ds1[c].astype(BF16)
                dke_ = _dot(v_[_rows(c)], dsb)
                dke_c.append(dke_)
                dv_c.append(_dot(ke_[_rows(c)], dsb, NT))
                dqd_c.append(_dot(do_[_rows(c)], st0.astype(BF16)))
                dt = (jnp.sum(ds1[c] * st0, axis=0, keepdims=True) * et[c * CH:c * CH + 1, sl]
                      + jnp.sum(dke_ * kef[_rows(c), sl], axis=0, keepdims=True))
                dtot_c.append(jnp.broadcast_to(dt, (CH, HGD)))
            dke = jnp.concatenate(dke_c, axis=0)
            dqd = dqd + jnp.concatenate(dqd_c, axis=0)
            dv_l.append(dv + jnp.concatenate(dv_c, axis=0))
            dtot_l.append(jnp.concatenate(dtot_c, axis=0))
            dq_l.append(dqd * ea[:, sl])
            dk_l.append(dkd * eb[:, sl] + dke * ee[:, sl])
            dcum_l.append(dqd * qdf[:, sl] - dkd * kdf[:, sl] - dke * kef[:, sl])
        dcum = jnp.concatenate(dcum_l, axis=1)
        dlf = _chunk_scan(dcum, rev, transpose=True) + jnp.concatenate(dtot_l, axis=1)
        dq_t = jnp.concatenate(dq_l, axis=1)
        dv_t = jnp.concatenate(dv_l, axis=1)

        df = dlf / f - jnp.concatenate(dk_l, axis=1)
        dfl = df * (1.0 - lb) * sg * (1.0 - sg)
        dlb = jnp.sum(df * (1.0 - sg), axis=0, keepdims=True)
        dl0 = dlb * lb * (1.0 - lb)
        dlg_ref[...] += jnp.concatenate([dl0, -dl0], axis=0)[None]
        if second:
            sq = _sig(qraw)
            dqr = (dqp_ref[...] + dq_t) * (HGD ** -0.5) * (sq * (1.0 + qraw * (1.0 - sq)))
            dp_ref[...] = jnp.concatenate([dfl, dvp_ref[...] + dv_t, dqr], axis=1).astype(BF16)
        else:
            dp_ref[...] = dfl.astype(BF16)
            dv_ref[...] = dv_t
            dq_ref[...] = dq_t

    def col(cb):
        return pl.BlockSpec((TM, HGW), lambda s: (tile_of(s), cb))

    tok = pl.BlockSpec((TM, HGW), lambda s: (tile_of(s), 0))
    in_specs = [col(C_FB if rev else C_FF), col(C_INP), col(C_QHG),
                pl.BlockSpec((1, 2, HGW), lambda s: (d, 0, 0)),
                pl.BlockSpec((TM, HGW), lambda s: (jnp.maximum(tile_of(s) - 1, 0), 0)),
                pl.BlockSpec((ncht, 4, HGD, HGD), lambda s: (tile_of(s), 0, 0, 0))]
    args = [p, p, p, lg, do, st]
    dlg_spec = _full((1, 2, HGW))
    dlg_shape = jax.ShapeDtypeStruct((1, 2, HGW), F32)
    if second:
        in_specs += [tok, tok]
        args += [prev[0], prev[1]]
        out_specs = [pl.BlockSpec((TM, 3 * HGW), lambda s: (tile_of(s), 0)), dlg_spec]
        out_shape = [jax.ShapeDtypeStruct(dp.shape, BF16), dlg_shape]
    else:
        out_specs = [pl.BlockSpec((TM, HGW), lambda s: (tile_of(s), C_FB if rev else C_FF)), tok, tok, dlg_spec]
        out_shape = [jax.ShapeDtypeStruct(dp.shape, BF16), jax.ShapeDtypeStruct((tt, HGW), F32),
                     jax.ShapeDtypeStruct((tt, HGW), F32), dlg_shape]
    in_specs.append(ANY)
    args.append(dp)
    return _pcall(body, name="hgrn_bwd_rev" if rev else "hgrn_bwd", grid=(nt,),
                  in_specs=in_specs, out_specs=out_specs, out_shape=out_shape,
                  scratch=[pltpu.VMEM((4, HGD, HGD), F32)],
                  aliases={len(args) - 1: 0}, carry=carry)(*args)


def _head_rms(o, w, nheads):
    outs = []
    for h in range(nheads):
        oh = o[:, h * HGD:(h + 1) * HGD]
        outs.append(oh * lax.rsqrt(jnp.mean(oh * oh, axis=-1, keepdims=True) + EPS))
    return jnp.concatenate(outs, axis=1)


def _readout_bwd(o0, o1, p, hw4, dy, dp, carry=None):
    tt = o0.shape[0]
    s_len = tt - L

    def body(o0_ref, o1_ref, g_ref, w_ref, dy_ref, _dp_in, dp_ref, do_ref, dw_ref):
        i = pl.program_id(0)

        @pl.when(i == 0)
        def _():
            dw_ref[...] = jnp.zeros_like(dw_ref)
            dp_ref[...] = jnp.zeros_like(dp_ref)

        @pl.when(i >= 1)
        def _():
            o = o0_ref[...] + o1_ref[...]
            g = g_ref[...]
            w = w_ref[...]
            sg = _sig(g)
            dy_ = dy_ref[...]
            dsw = dy_ * (g * sg)
            outs, xhs = [], []
            for h in range(4):
                sl = slice(h * HGD, (h + 1) * HGD)
                oh = o[:, sl]
                r = lax.rsqrt(jnp.mean(oh * oh, axis=-1, keepdims=True) + EPS)
                xh = oh * r
                dxh = dsw[:, sl] * w[:, sl]
                outs.append(r * (dxh - xh * jnp.mean(dxh * xh, axis=-1, keepdims=True)))
                xhs.append(xh)
            xh = jnp.concatenate(xhs, axis=1)
            do_ref[...] = jnp.concatenate(outs, axis=1)
            dp_ref[...] = (dy_ * xh * w * (sg * (1.0 + g * (1.0 - sg)))).astype(BF16)
            dw_ref[...] += jnp.sum(dsw * xh, axis=0, keepdims=True)

    tok = pl.BlockSpec((TM, HGW), lambda i: (i, 0))
    lat = pl.BlockSpec((TM, HGW), lambda i: (jnp.maximum(i - 1, 0), 0))
    return _pcall(body, name="readout_bwd", grid=(tt // TM,),
                  in_specs=[tok, tok, pl.BlockSpec((TM, HGW), lambda i: (i, C_GHG)), _full((1, HGW)), lat, ANY],
                  out_specs=[pl.BlockSpec((TM, HGW), lambda i: (i, C_GHG)), lat, _full((1, HGW))],
                  out_shape=[jax.ShapeDtypeStruct(dp.shape, BF16), jax.ShapeDtypeStruct((s_len, HGW), F32),
                             jax.ShapeDtypeStruct((1, HGW), F32)],
                  aliases={5: 0}, carry=carry)(o0, o1, p, hw4, dy, dp)


def _rope_tables(s_len):
    t = np.arange(s_len)
    inv = ROPE_THETA ** (-np.arange(0, 32, 2, dtype=np.float64) / 32)
    def half(pos):
        ang = pos[:, None].astype(np.float64) * inv[None, :]
        return (np.concatenate([np.cos(ang), np.cos(ang)], 1), np.concatenate([-np.sin(ang), np.sin(ang)], 1))
    cr, sr = half(t // GRID_W)
    cc, sc = half(t % GRID_W)
    cos = np.concatenate([cr, cc, cr, cc], 1)
    sin = np.concatenate([sr, sc, sr, sc], 1)
    cos = np.concatenate([np.ones((L, 128)), cos], 0)
    sin = np.concatenate([np.zeros((L, 128)), sin], 0)
    return jnp.asarray(cos, F32), jnp.asarray(sin, F32)


def _blockdiag(n, w):
    i = np.arange(n)
    return jnp.asarray((i[:, None] // w == i[None, :] // w) / float(w), F32)


def _dup_matrix():
    m = np.zeros((128, 512), np.float32)
    for g in range(2):
        for j in range(4):
            for dd in range(HDIM):
                m[64 * g + dd, 256 * g + 64 * j + dd] = 1.0
    return m


def _head_mean(x, blockdiag):
    return _dot(x, blockdiag, prec=lax.Precision.HIGH)


def _rot(x):
    n = x.shape[1]
    lane = lax.broadcasted_iota(jnp.int32, x.shape, 1)
    return jnp.where((lane % 32) < 16, pltpu.roll(x, n - 16, 1), pltpu.roll(x, 16, 1))


def _qk_prep(p, cos, sin, qnw8, knw2, bd512, bd128, dup):
    tt = p.shape[0]

    def body(q_ref, kv_ref, cos_ref, sin_ref, qw_ref, kw_ref, b5_ref, b1_ref, dup_ref,
             qr_ref, k4_ref, v4_ref):
        cos_, sin_ = cos_ref[...], sin_ref[...]
        q = q_ref[...]
        qn = q * lax.rsqrt(_head_mean(q * q, b5_ref[...]) + EPS) * qw_ref[...]
        cos4 = jnp.concatenate([cos_] * 4, axis=1)
        sin4 = jnp.concatenate([sin_] * 4, axis=1)
        qr_ref[...] = ((qn * cos4 + _rot(qn) * sin4) * (HDIM ** -0.5)).astype(BF16)
        kv = kv_ref[...]
        k, v = kv[:, :128], kv[:, 128:]
        kn = k * lax.rsqrt(_head_mean(k * k, b1_ref[...]) + EPS) * kw_ref[...]
        kr = kn * cos_ + _rot(kn) * sin_
        k4_ref[...] = _bdot(kr, dup_ref[...]).astype(BF16)
        v4_ref[...] = _bdot(v, dup_ref[...]).astype(BF16)

    row = lambda w, cb: pl.BlockSpec((TM, w), lambda i: (i, cb))
    out = jax.ShapeDtypeStruct((tt, ATW), BF16)
    return _pcall(body, name="qk_prep", grid=(tt // TM,),
                  in_specs=[row(ATW, C_QRAW), row(256, C_KV), row(128, 0), row(128, 0),
                            _full((1, ATW)), _full((1, 128)), _full((ATW, ATW)), _full((128, 128)),
                            _full((128, ATW))],
                  out_specs=[row(ATW, 0)] * 3, out_shape=[out] * 3)(
                      p, p, cos, sin, qnw8, knw2, bd512, bd128, dup)


def _attn_masks(i, nb):
    r = lax.broadcasted_iota(jnp.int32, (4 * BLK, 3 * BLK + L), 0) % BLK
    c = lax.broadcasted_iota(jnp.int32, (4 * BLK, 3 * BLK + L), 1)
    kpos = (i - 1) * BLK + c
    loc = (jnp.abs(c - BLK - r) <= BLK) & (kpos >= 0) & (kpos < nb * BLK)
    return loc | (c >= 3 * BLK)


def _stack_mask():
    r = lax.broadcasted_iota(jnp.int32, (4 * BLK, 256), 0)
    lane = lax.broadcasted_iota(jnp.int32, (4 * BLK, 256), 1)
    return (r // BLK) == (lane // HDIM)


def _stack_heads(xg, fill=0.0):
    x4 = jnp.concatenate([xg] * 4, axis=0)
    return jnp.where(_stack_mask(), x4, jnp.full_like(x4, fill))


def _unstack_heads(x4):
    out = jnp.where(_lane_mask(0), x4[0:BLK], 0.0)
    for j in range(1, 4):
        out = out + jnp.where(_lane_mask(j), x4[j * BLK:(j + 1) * BLK], 0.0)
    return out


def _per_head_rows(vals):
    return jnp.concatenate([jnp.broadcast_to(v, (BLK, 1)) for v in vals], axis=0)


def _lane_mask(j):
    lane = lax.broadcasted_iota(jnp.int32, (1, 256), 1)
    return (lane // HDIM) == j


def _attn_specs(nb):
    blk = lambda off: pl.BlockSpec((BLK, ATW), lambda i: (jnp.clip(i + off, 0, nb - 1) + 2, 0))
    ctx = pl.BlockSpec((L, ATW), lambda i: (0, 0))
    return blk, ctx


def _attn_fwd(qr, k4, v4, sinks, carry=None):
    tt = qr.shape[0]
    s_len = tt - L
    nb = s_len // BLK

    def body(sk_ref, q_ref, kp, ko, kn, kc, vp, vo, vn, vc, y_ref, lse_ref):
        i = pl.program_id(0)
        valid = _attn_masks(i, nb)
        q = q_ref[...]
        ys, lses = [], []
        for g in range(2):
            gs = slice(256 * g, 256 * g + 256)
            kcat = jnp.concatenate([kp[:, gs], ko[:, gs], kn[:, gs], kc[:, gs]], axis=0)
            vcat = jnp.concatenate([vp[:, gs], vo[:, gs], vn[:, gs], vc[:, gs]], axis=0)
            sink4 = _per_head_rows([sk_ref[4 * g + j] for j in range(4)])
            q4 = _stack_heads(q[:, gs])
            o_parts, l_parts = [], []
            for hp in range(2):
                rows = slice(2 * BLK * hp, 2 * BLK * (hp + 1))
                sink = sink4[rows]
                s = jnp.where(valid[rows], _dot(q4[rows], kcat, NT), -1e30)
                m = jnp.maximum(jnp.max(s, axis=-1, keepdims=True), sink)
                e = jnp.exp(s - m)
                den = jnp.sum(e, axis=-1, keepdims=True) + jnp.exp(sink - m)
                o_parts.append(_bdot(e * (1.0 / den), vcat))
                l_parts.append(jnp.broadcast_to(m + jnp.log(den), (2 * BLK, 256)))
            ys.append(_unstack_heads(jnp.concatenate(o_parts, axis=0)))
            lses.append(_unstack_heads(jnp.concatenate(l_parts, axis=0)))
        y_ref[...] = jnp.concatenate(ys, axis=1).astype(BF16)
        lse_ref[...] = jnp.concatenate(lses, axis=1)

    blk, ctx = _attn_specs(nb)
    out = pl.BlockSpec((BLK, ATW), lambda i: (i, 0))
    return _pcall(body, name="attn_fwd", grid=(nb,),
                  in_specs=[pl.BlockSpec(memory_space=pltpu.SMEM), blk(0),
                            blk(-1), blk(0), blk(1), ctx, blk(-1), blk(0), blk(1), ctx],
                  out_specs=[out, out],
                  out_shape=[jax.ShapeDtypeStruct((s_len, ATW), BF16),
                             jax.ShapeDtypeStruct((s_len, ATW), F32)], carry=carry)(
                      sinks, qr, k4, k4, k4, k4, v4, v4, v4, v4)


def _attn_bwd(qr, k4, v4, sinks, y, lse, dy, carry=None):
    tt = qr.shape[0]
    s_len = tt - L
    nb = s_len // BLK

    def body(sk_ref, q_ref, kp, ko, kn, kc, vp, vo, vn, vc, y_ref, lse_ref, dy_ref,
             dq_ref, dkw_ref, dvw_ref, dkc_ref, dvc_ref, dsk_ref):
        i = pl.program_id(0)

        @pl.when(i == 0)
        def _():
            dkc_ref[...] = jnp.zeros_like(dkc_ref)
            dvc_ref[...] = jnp.zeros_like(dvc_ref)
            dsk_ref[...] = jnp.zeros_like(dsk_ref)

        valid = _attn_masks(i, nb)
        q = q_ref[...]
        dy_ = dy_ref[...]
        dly = dy_ * y_ref[...].astype(F32)
        lse_ = lse_ref[...]
        dqs = []
        for g in range(2):
            gs = slice(256 * g, 256 * g + 256)
            kcat = jnp.concatenate([kp[:, gs], ko[:, gs], kn[:, gs], kc[:, gs]], axis=0)
            vcat = jnp.concatenate([vp[:, gs], vo[:, gs], vn[:, gs], vc[:, gs]], axis=0)
            q4 = _stack_heads(q[:, gs])
            dy4 = _stack_heads(dy_[:, gs]).astype(BF16)
            lse4 = jnp.max(_stack_heads(lse_[:, gs], fill=-1e30), axis=-1, keepdims=True)
            delta = jnp.sum(_stack_heads(dly[:, gs]), axis=-1, keepdims=True)
            sink = _per_head_rows([sk_ref[4 * g + j] for j in range(4)])
            pr = jnp.where(valid, jnp.exp(_dot(q4, kcat, NT) - lse4), 0.0)
            dsb = (pr * (_dot(dy4, vcat, NT) - delta)).astype(BF16)
            dsink = jnp.exp(sink - lse4) * delta
            for j in range(4):
                dsk_ref[4 * g + j:4 * g + j + 1, :] += jnp.broadcast_to(
                    -jnp.sum(dsink[j * BLK:(j + 1) * BLK], axis=0, keepdims=True), (1, 128))
            dqs.append(_unstack_heads(_dot(dsb, kcat)))
            dkg = _dot(dsb, q4, TN)
            dvg = _dot(pr.astype(BF16), dy4, TN)
            dkw_ref[0, :, gs] = dkg[:3 * BLK]
            dvw_ref[0, :, gs] = dvg[:3 * BLK]
            dkc_ref[:, gs] += dkg[3 * BLK:]
            dvc_ref[:, gs] += dvg[3 * BLK:]
        dq_ref[...] = jnp.concatenate(dqs, axis=1)

    blk, ctx = _attn_specs(nb)
    out = pl.BlockSpec((BLK, ATW), lambda i: (i, 0))
    win = pl.BlockSpec((1, 3 * BLK, ATW), lambda i: (i, 0, 0))
    acc = _full((L, ATW))
    return _pcall(body, name="attn_bwd", grid=(nb,),
                  in_specs=[pl.BlockSpec(memory_space=pltpu.SMEM), blk(0),
                            blk(-1), blk(0), blk(1), ctx, blk(-1), blk(0), blk(1), ctx, out, out, out],
                  out_specs=[out, win, win, acc, acc, _full((8, 128))],
                  out_shape=[jax.ShapeDtypeStruct((s_len, ATW), F32),
                             jax.ShapeDtypeStruct((nb, 3 * BLK, ATW), F32),
                             jax.ShapeDtypeStruct((nb, 3 * BLK, ATW), F32),
                             jax.ShapeDtypeStruct((L, ATW), F32), jax.ShapeDtypeStruct((L, ATW), F32),
                             jax.ShapeDtypeStruct((8, 128), F32)], carry=carry)(
                      sinks, qr, k4, k4, k4, k4, v4, v4, v4, v4, y, lse, dy)


def _attn_post(p, cos, sin, qnw8, knw2, bd512, bd128, dupt, dq, dkw, dvw, dkc, dvc, dp, carry=None):
    tt = p.shape[0]
    s_len = tt - L
    nb = s_len // BLK
    nctx = L // BLK

    def body(q_ref, kv_ref, cos_ref, sin_ref, qw_ref, kw_ref, b5_ref, b1_ref, dupt_ref,
             dq_ref, kwp, kwo, kwn, vwp, vwo, vwn, dkc_ref, dvc_ref, _dp_in,
             dp_ref, dqw_ref, dkw_ref):
        t = pl.program_id(0)
        j = t - nctx

        @pl.when(t == 0)
        def _():
            dqw_ref[...] = jnp.zeros_like(dqw_ref)
            dkw_ref[...] = jnp.zeros_like(dkw_ref)

        is_lat = t >= nctx
        cos_, sin_ = cos_ref[...], sin_ref[...]
        has_p = is_lat & (j >= 1)
        has_n = is_lat & (j <= nb - 2)
        dk4 = (jnp.where(is_lat, kwo[0], dkc_ref[...]) + jnp.where(has_p, kwp[0], 0.0)
               + jnp.where(has_n, kwn[0], 0.0))
        dv4 = (jnp.where(is_lat, vwo[0], dvc_ref[...]) + jnp.where(has_p, vwp[0], 0.0)
               + jnp.where(has_n, vwn[0], 0.0))
        dkr = _dot(dk4, dupt_ref[...], prec=HI)
        dv = _dot(dv4, dupt_ref[...], prec=HI)
        kv = kv_ref[...]
        k = kv[:, :128]
        kw = kw_ref[...]
        rk = lax.rsqrt(_head_mean(k * k, b1_ref[...]) + EPS)
        xk = k * rk
        dkn = dkr * cos_ + _rot(dkr * sin_)
        dxk = dkn * kw
        dk = rk * (dxk - xk * _head_mean(dxk * xk, b1_ref[...]))
        dkw_ref[...] += jnp.sum(dkn * xk, axis=0, keepdims=True)
        q = q_ref[...]
        qw = qw_ref[...]
        rq = lax.rsqrt(_head_mean(q * q, b5_ref[...]) + EPS)
        xq = q * rq
        cos4 = jnp.concatenate([cos_] * 4, axis=1)
        sin4 = jnp.concatenate([sin_] * 4, axis=1)
        dqr = jnp.where(is_lat, dq_ref[...], 0.0) * (HDIM ** -0.5)
        dqn = dqr * cos4 + _rot(dqr * sin4)
        dxq = dqn * qw
        dqraw = rq * (dxq - xq * _head_mean(dxq * xq, b5_ref[...]))
        dqw_ref[...] += jnp.sum(dqn * xq, axis=0, keepdims=True)
        dp_ref[...] = jnp.concatenate([dqraw, dk, dv], axis=1).astype(BF16)

    row = lambda w, cb: pl.BlockSpec((BLK, w), lambda t: (t, cb))
    lat = pl.BlockSpec((BLK, ATW), lambda t: (jnp.maximum(t - nctx, 0), 0))

    def part(off):
        return pl.BlockSpec((1, BLK, ATW), lambda t: (jnp.clip(t - nctx + off, 0, nb - 1), 1 - off, 0))

    cacc = pl.BlockSpec((BLK, ATW), lambda t: (jnp.minimum(t, nctx - 1), 0))
    return _pcall(body, name="attn_post", grid=(tt // BLK,),
                  in_specs=[row(ATW, C_QRAW), row(256, C_KV), row(128, 0), row(128, 0),
                            _full((1, ATW)), _full((1, 128)), _full((ATW, ATW)), _full((128, 128)),
                            _full((ATW, 128)), lat, part(-1), part(0), part(1), part(-1), part(0), part(1),
                            cacc, cacc, ANY],
                  out_specs=[pl.BlockSpec((BLK, 768), lambda t: (t, C_QKV)), _full((1, ATW)), _full((1, 128))],
                  out_shape=[jax.ShapeDtypeStruct(dp.shape, BF16), jax.ShapeDtypeStruct((1, ATW), F32),
                             jax.ShapeDtypeStruct((1, 128), F32)],
                  aliases={18: 0}, carry=carry)(p, p, cos, sin, qnw8, knw2, bd512, bd128, dupt,
                                   dq, dkw, dkw, dkw, dvw, dvw, dvw, dkc, dvc, dp)


def _branch_merge(y_hg, y_at, bh4, ba4, p):
    s_len = y_hg.shape[0]

    def body(yh_ref, ya_ref, bh_ref, ba_ref, gh_ref, ga_ref, ah_ref, aa_ref, m_ref):
        yh, ya = yh_ref[...], ya_ref[...]
        ah = jnp.concatenate([_bdot(yh, bh_ref[j]) for j in range(4)], axis=1)
        aa = jnp.concatenate([_bdot(ya, ba_ref[j]) for j in range(4)], axis=1)
        ah_ref[...] = ah.astype(BF16)
        aa_ref[...] = aa.astype(BF16)
        m_ref[...] = (_sig(gh_ref[...]) * ah + _sig(ga_ref[...]) * aa).astype(BF16)

    row = pl.BlockSpec((TM, D), lambda i: (i, 0))
    y = pl.BlockSpec((TM, HGW), lambda i: (i, 0))
    f = jax.ShapeDtypeStruct((s_len, D), BF16)
    return _pcall(body, name="branch_merge", grid=(s_len // TM,),
                  in_specs=[y, y, _full(bh4.shape), _full(ba4.shape),
                            pl.BlockSpec((TM, D), lambda i: (i + 1, 2)), pl.BlockSpec((TM, D), lambda i: (i + 1, 3))],
                  out_specs=[row, row, row],
                  out_shape=[f, f, jax.ShapeDtypeStruct((s_len, D), BF16)])(y_hg, y_at, bh4, ba4, p, p)


def _branch_bwd(dmh, dma, bh4, ba4, y_hg, y_at):
    s_len = dmh.shape[0]
    nk = s_len // TS
    ns = D // 4

    def body(dh_ref, da_ref, bh_ref, ba_ref, yh_ref, ya_ref, dyh_ref, dya_ref, gh_ref, ga_ref, acc_h, acc_a):
        t = pl.program_id(0)

        @pl.when(t == 0)
        def _():
            acc_h[...] = jnp.zeros_like(acc_h)
            acc_a[...] = jnp.zeros_like(acc_a)

        for d_ref, w_ref, y_ref, dy_ref, acc in ((dh_ref, bh_ref, yh_ref, dyh_ref, acc_h),
                                                 (da_ref, ba_ref, ya_ref, dya_ref, acc_a)):
            y = y_ref[...]
            dy = jnp.zeros((TS, HGW), F32)
            for j in range(4):
                dj = d_ref[:, j * ns:(j + 1) * ns]
                dy = dy + _bdot(dj, w_ref[j], NT)
                acc[j] += _bdot(y, dj, TN)
            dy_ref[...] = dy

        @pl.when(t == nk - 1)
        def _():
            gh_ref[...] = acc_h[...].astype(BF16)
            ga_ref[...] = acc_a[...].astype(BF16)

    dm = pl.BlockSpec((TS, D), lambda t: (t, 0))
    y = pl.BlockSpec((TS, HGW), lambda t: (t, 0))
    w = _full(bh4.shape)
    fy = jax.ShapeDtypeStruct((s_len, HGW), F32)
    gw = jax.ShapeDtypeStruct(bh4.shape, BF16)
    return _pcall(body, name="branch_bwd", grid=(nk,), in_specs=[dm, dm, w, w, y, y],
                  out_specs=[y, y, w, w], out_shape=[fy, fy, gw, gw],
                  scratch=[pltpu.VMEM(bh4.shape, F32)] * 2)(dmh, dma, bh4, ba4, y_hg, y_at)


def _merge_bwd(dattn, w_o, mixed, ah, aa, p, carry=None):
    tt = p.shape[0]
    s_len = tt - L
    nt = tt // TM

    def body(da_ref, wo_ref, mx_ref, ah_ref, aa_ref, gh_ref, ga_ref, dp_ref, dmh_ref, dma_ref, go_ref, acc):
        i = pl.program_id(0)

        @pl.when(i == 0)
        def _():
            dp_ref[...] = jnp.zeros_like(dp_ref)
            acc[...] = jnp.zeros_like(acc)

        @pl.when(i >= 1)
        def _():
            da = da_ref[...]
            acc[...] += _bdot(mx_ref[...], da, TN)
            dm_ = _bdot(da, wo_ref[...], NT)
            sh, sa = _sig(gh_ref[...]), _sig(ga_ref[...])
            dp_ref[...] = jnp.concatenate([dm_ * ah_ref[...].astype(F32) * sh * (1.0 - sh),
                                           dm_ * aa_ref[...].astype(F32) * sa * (1.0 - sa)], axis=1).astype(BF16)
            dmh_ref[...] = (dm_ * sh).astype(BF16)
            dma_ref[...] = (dm_ * sa).astype(BF16)

        @pl.when(i == nt - 1)
        def _():
            go_ref[...] = acc[...].astype(BF16)

    lat = pl.BlockSpec((TM, D), lambda i: (jnp.maximum(i - 1, 0), 0))
    return _pcall(body, name="merge_bwd", grid=(nt,),
                  in_specs=[lat, _full((D, D)), lat, lat, lat, pl.BlockSpec((TM, D), lambda i: (i, 2)),
                            pl.BlockSpec((TM, D), lambda i: (i, 3))],
                  out_specs=[pl.BlockSpec((TM, 2 * D), lambda i: (i, C_GATES)), lat, lat, _full((D, D))],
                  out_shape=[jax.ShapeDtypeStruct((tt, NCOL), BF16), jax.ShapeDtypeStruct((s_len, D), BF16),
                             jax.ShapeDtypeStruct((s_len, D), BF16), jax.ShapeDtypeStruct((D, D), BF16)],
                  scratch=[pltpu.VMEM((D, D), F32)], carry=carry)(dattn, w_o, mixed, ah, aa, p, p)


def _local_step(x, ctx, tgt, mod, modc, nw1, nw2, lg, hw, qnw, knw, sinks,
                w_in, wts, dist=None):
    s_len = x.shape[0]
    tt = s_len + L
    ss1 = jnp.stack([modc, mod[0:2]])
    ss2 = mod[3:5][None]
    g1, g2 = mod[2:3], mod[5:6]
    hw4 = jnp.tile(hw, (1, 4))
    qnw8 = jnp.tile(qnw, (1, 8))
    knw2 = jnp.tile(knw, (1, 2))
    cos, sin = _rope_tables(s_len)
    bd512, bd128 = _blockdiag(ATW, HDIM), _blockdiag(128, HDIM)
    dupm = _dup_matrix()
    dup, dupt = jnp.asarray(dupm, BF16), jnp.asarray(dupm.T, F32)
    tmt = tt

    def four(b):
        return b.reshape(4, 2 * b.shape[1], b.shape[2])

    def halves(g):
        return g.reshape(4, 2, g.shape[1] // 2, g.shape[2])

    h = _mod1(ctx, x, nw1, ss1)
    if dist is None:
        bh4, ba4, w_o, g4, u4, dn4 = wts
        p = _mm_in(h, w_in, tmt)
        o0, st0 = _hgrn_fwd(p, lg, rev=False)
        o1, st1, y_hg = _hgrn_fwd(p, lg, rev=True, readout=(o0, hw4))
    else:
        core, chip = dist
        half = wts[3].shape[1] // 2
        p, first = _mm_in(h, w_in, tmt, carry=_carry_join(_carry_gather(list(wts[0:3])),
                                                          _carry_gather([wts[3]], rows=[(0, half)])))
        (o0, st0), (g8,) = _hgrn_fwd(p, lg, rev=False, carry=_carry_gather([first[3]], rows=[(half, half)]))
        (o1, st1, y_hg), (dn8a,) = _hgrn_fwd(p, lg, rev=True, readout=(o0, hw4),
                                             carry=_carry_gather([wts[5]], rows=[(0, half)]))
        bh4, ba4, w_o, g4 = four(first[0]), four(first[1]), four(first[2]).reshape(D, D), four(g8)
    qr, k4, v4 = _qk_prep(p, cos, sin, qnw8, knw2, bd512, bd128, dup)
    if dist is None:
        y_at, lse = _attn_fwd(qr, k4, v4, sinks)
    else:
        (y_at, lse), (u8, dn8) = _attn_fwd(qr, k4, v4, sinks,
                                           carry=_carry_gather([wts[4], dn8a], rows=[None, (half, half)]))
        u4, dn4 = four(u8), four(dn8)
    ah, aa, mixed = _branch_merge(y_hg, y_at, bh4, ba4, p)
    ao, x1, h2 = _out_proj_mod2(mixed, w_o, x, g1, nw2, ss2)
    a4, b4, z4 = _ffn_up(h2, g4, u4)
    sq, dx2, dyb, dg2 = _ffn_down_loss(z4, dn4, x1, g2, tgt)

    da4, db4 = _ffn_dz(dyb, dn4, a4, b4)
    g_dn = _ffn_gdn(z4, dyb)
    if dist is None:
        dh2 = _ffn_dh2(da4, db4, g4, u4)
    else:
        dn_units = [halves(g_dn)]
        dh2, dn_recv = _ffn_dh2(da4, db4, g4, u4, carry=_carry_pairx(dn_units))
        dn_pairs = _rs_pair_add(dn_units, dn_recv, core)
    if dist is None:
        g_g, g_u = _ffn_ggu(h2, da4, db4)
    else:
        (g_g, g_u), c_dn = _ffn_ggu(h2, da4, db4, carry=_carry_chipx(dn_pairs))
        red_dn = _rs_chip_add(dn_pairs, c_dn, core, chip)
    dx1, dattn, dss2, dnw2, dg1 = _mod2_bwd(x1, dh2, dx2, ao, nw2, ss2, g1)
    if dist is None:
        dp, dmh, dma, g_o = _merge_bwd(dattn, w_o, mixed, ah, aa, p)
    else:
        gu_units = [halves(g_g), halves(g_u)]
        (dp, dmh, dma, g_o), gu_recv = _merge_bwd(dattn, w_o, mixed, ah, aa, p, carry=_carry_pairx(gu_units))
        ffn_pairs = list(dn_pairs) + list(_rs_pair_add(gu_units, gu_recv, core))
    dy_hg, dy_at, g_bh, g_ba = _branch_bwd(dmh, dma, bh4, ba4, y_hg, y_at)
    if dist is None:
        dp, do, dhw4 = _readout_bwd(o0, o1, p, hw4, dy_hg, dp)
        dq, dkw, dvw, dkc, dvc, dsk = _attn_bwd(qr, k4, v4, sinks, y_at, lse, dy_at)
        dp, dqnw8, dknw2 = _attn_post(p, cos, sin, qnw8, knw2, bd512, bd128, dupt, dq, dkw, dvw, dkc, dvc, dp)
    else:
        mix_units = [halves(g_bh), halves(g_ba), halves(g_o.reshape(4, D // 4, D))]
        (dp, do, dhw4), mix_recv = _readout_bwd(o0, o1, p, hw4, dy_hg, dp, carry=_carry_pairx(mix_units))
        mix_pairs = _rs_pair_add(mix_units, mix_recv, core)
        (dq, dkw, dvw, dkc, dvc, dsk), bwd = _attn_bwd(
            qr, k4, v4, sinks, y_at, lse, dy_at,
            carry=_carry_join(_carry_chipx(ffn_pairs[1:2]), _carry_sibx(red_dn)))
        red_g = _rs_chip_add(ffn_pairs[1:2], bwd[0:1], core, chip)
        (dp, dqnw8, dknw2), post = _attn_post(
            p, cos, sin, qnw8, knw2, bd512, bd128, dupt, dq, dkw, dvw, dkc, dvc, dp, carry=_carry_sibx(red_g))
    if dist is None:
        dp, dv0, dq0, dlg0 = _hgrn_bwd(p, lg, do, st0, dp, None, rev=False)
        dp, dlg1 = _hgrn_bwd(p, lg, do, st1, dp, (dv0, dq0), rev=True)
    else:
        (dp, dv0, dq0, dlg0), c_u = _hgrn_bwd(p, lg, do, st0, dp, None, rev=False,
                                              carry=_carry_chipx(ffn_pairs[2:3]))
        red_u = _rs_chip_add(ffn_pairs[2:3], c_u, core, chip)
        (dp, dlg1), last = _hgrn_bwd(p, lg, do, st1, dp, (dv0, dq0), rev=True,
                                     carry=_carry_join(_carry_chipx(mix_pairs), _carry_sibx(red_u)))
        mix_reds = _rs_chip_add(mix_pairs, last[0:3], core, chip)
        ffn_done = bwd[1:2] + post[0:1] + last[3:4]
    g_in = _mm_gin(dp, h, tmt)
    if dist is None:
        dh = _mm_dh(dp, w_in, tmt)
        gx, dss1, dnw1 = _mod1_bwd(ctx, x, dh, dx1, nw1, ss1)
        rs = None
    else:
        in_units = [halves(g_in.reshape(4, NCOL // 4, D))]
        in_pairs = _rs_pair_add(in_units, _pair_exchange(in_units), core)
        first_rows, rest_rows = _split_rows(in_pairs[0].shape[1], IN_ROWS_WITH_DH)
        dh, both = _mm_dh(dp, w_in, tmt, carry=_carry_join(_carry_chipx(in_pairs, rows=first_rows),
                                                           _carry_sibx(mix_reds)))
        in_part, mix_done = both[0:1], both[1:4]
        gx, dss1, dnw1 = _mod1_bwd(ctx, x, dh, dx1, nw1, ss1)
        rs = dict(ffn_done=ffn_done, mix_done=mix_done, in_pairs=in_pairs, in_part=in_part, rest_rows=rest_rows)

    dmod = jnp.concatenate([dss1[1], dg1, dss2, dg2], axis=0)
    dmodc = dss1[0]
    raw = (dss1, dg1, dss2, dg2, dnw1, dnw2, dhw4, dqnw8, dknw2, dsk, dlg0, dlg1)
    small = dict(raw=raw, dmod=dmod, dmodc=dmodc, dnw1=dnw1, dnw2=dnw2,
                 dhw=dhw4.reshape(4, HGD).sum(0, keepdims=True),
                 dqnw=dqnw8.reshape(8, HDIM).sum(0, keepdims=True),
                 dknw=dknw2.reshape(2, HDIM).sum(0, keepdims=True),
                 dsinks=dsk[:, 0], dlg=jnp.concatenate([dlg0, dlg1], axis=0))
    big = dict(w_in=g_in, w_bh=g_bh, w_ba=g_ba, w_o=g_o, w_g=g_g, w_u=g_u, w_dn=g_dn)
    return sq, gx, big, small, rs


def _place():
    x, y, c = lax.axis_index("x"), lax.axis_index("y"), lax.axis_index("c")
    return x, y, c


def _gather_blocks(x_refs, out_refs, send_sems, recv_sems, local_sems):
    n = len(out_refs)
    x, y, c = _place()
    me, sibling = (x, y, c), (x, y, 1 - c)
    chips = [(1 - x, y), (x, 1 - y), (1 - x, 1 - y)]

    def slot(u, px, py, pc):
        return out_refs[u].at[4 * px + 2 * py + pc]

    def copy(u, k, block, to, src=None):
        return pltpu.make_async_remote_copy(
            src_ref=slot(u, *block) if src is None else src, dst_ref=slot(u, *block),
            send_sem=send_sems.at[u, k], recv_sem=recv_sems.at[u, k], device_id=to, device_id_type=MESH)

    mines = [pltpu.make_async_copy(x_refs[u], slot(u, *me), local_sems.at[u]) for u in range(n)]
    for cp in mines:
        cp.start()
    first = []
    for u in range(n):
        first.append(copy(u, 0, me, sibling, src=x_refs[u]))
        first += [copy(u, 1 + j, me, (*chip, c), src=x_refs[u]) for j, chip in enumerate(chips)]
    for cp in first:
        cp.start()
    passed = []
    for j, chip in enumerate(chips):
        for u in range(n):
            copy(u, 1 + j, (*chip, c), me).wait_recv()
            fwd = copy(u, 4 + j, (*chip, c), sibling)
            fwd.start()
            passed.append(fwd)
    for u in range(n):
        copy(u, 0, sibling, me).wait_recv()
    for j, chip in enumerate(chips):
        for u in range(n):
            copy(u, 4 + j, (*chip, 1 - c), me).wait_recv()
    for cp in first + passed:
        cp.wait_send()
    for cp in mines:
        cp.wait()


def _gather_sems(n):
    return [pltpu.SemaphoreType.DMA((n, 7)), pltpu.SemaphoreType.DMA((n, 7)), pltpu.SemaphoreType.DMA((n,))]


def _allgather(blks, *, name):
    n = len(blks)
    vm = pl.BlockSpec(memory_space=pltpu.VMEM)

    def body(*refs):
        _peer_barrier("both")
        _gather_blocks(refs[:n], refs[n:2 * n], *refs[2 * n:])

    return pl.pallas_call(
        body, name=name, out_shape=[jax.ShapeDtypeStruct((8,) + b.shape, b.dtype) for b in blks],
        in_specs=[vm] * n, out_specs=[vm] * n, scratch_shapes=_gather_sems(n),
        compiler_params=pltpu.CompilerParams(collective_id=BARRIER_IDS["both"]))(*blks)


def _cast_place(ws, c, dev):
    n = len(ws)

    def body(s_ref, *refs):
        for u in range(n):
            refs[n + u][0] = refs[u][...].astype(BF16)

    in_specs, out_specs, out_shape = [], [], []
    for w in ws:
        q, cols = w.shape[0] // 4, w.shape[1]
        in_specs.append(pl.BlockSpec((q, cols), lambda i, s: (2 * s[0] + i, 0)))
        out_specs.append(pl.BlockSpec((1, q, cols), lambda i, s: (s[1], i, 0)))
        out_shape.append(jax.ShapeDtypeStruct((8, 2 * q, cols), BF16))
    return pl.pallas_call(
        body, name="cast_place",
        grid_spec=pltpu.PrefetchScalarGridSpec(num_scalar_prefetch=1, grid=(2,), in_specs=in_specs,
                                               out_specs=out_specs),
        out_shape=_out_hbm(out_shape),
        compiler_params=pltpu.CompilerParams(vmem_limit_bytes=48 << 20))(jnp.stack([c, dev]), *_in_hbm(ws))


def _gather_phases(out_refs, send_sems, recv_sems, rows=None):
    n = len(out_refs)
    x, y, c = _place()
    me, sibling = (x, y, c), (x, y, 1 - c)
    chips = [(1 - x, y), (x, 1 - y), (1 - x, 1 - y)]

    def copy(u, k, block, to):
        px, py, pc = block
        ref = out_refs[u].at[4 * px + 2 * py + pc]
        if rows is not None and rows[u] is not None:
            ref = ref.at[pl.ds(rows[u][0], rows[u][1])]
        return pltpu.make_async_remote_copy(src_ref=ref, dst_ref=ref, send_sem=send_sems.at[u, k],
                                            recv_sem=recv_sems.at[u, k], device_id=to, device_id_type=MESH)

    def start():
        for u in range(n):
            copy(u, 0, me, sibling).start()
            for j, chip in enumerate(chips):
                copy(u, 1 + j, me, (*chip, c)).start()

    def mid():
        for j, chip in enumerate(chips):
            for u in range(n):
                copy(u, 1 + j, (*chip, c), me).wait_recv()
                copy(u, 4 + j, (*chip, c), sibling).start()

    def end():
        for u in range(n):
            copy(u, 0, sibling, me).wait_recv()
        for j, chip in enumerate(chips):
            for u in range(n):
                copy(u, 4 + j, (*chip, 1 - c), me).wait_recv()
        for u in range(n):
            copy(u, 0, me, sibling).wait_send()
            for j, chip in enumerate(chips):
                copy(u, 1 + j, me, (*chip, c)).wait_send()
                copy(u, 4 + j, (*chip, c), sibling).wait_send()

    return start, mid, end


def _carry_gather(bufs, rows=None):
    n = len(bufs)
    return _Carry(bufs, [jax.ShapeDtypeStruct(b.shape, b.dtype) for b in bufs], {u: u for u in range(n)},
                  [pltpu.SemaphoreType.DMA((n, 7)), pltpu.SemaphoreType.DMA((n, 7))],
                  lambda ins, outs, sems: _gather_phases(outs, *sems, rows=rows), "both")


def _ag_small(raw, sq):
    def body(dss1, dg1, dss2, dg2, dnw1, dnw2, dhw4, dqnw8, dknw2, dsk, dlg0, dlg1, sq_ref,
             out_ref, tot_ref, blk, send_sems, recv_sems, local_sems):
        _peer_barrier("both")
        blk[...] = jnp.zeros_like(blk)
        blk[0:2, :] = dss1[1]
        blk[2:3, :] = dg1[...]
        blk[3:5, :] = dss2[...]
        blk[5:6, :] = dg2[...]
        blk[6:8, :] = dss1[0]
        blk[8:9, :] = dnw1[...]
        blk[9:10, :] = dnw2[...]
        blk[10:11, 0:HGW] = dhw4[...]
        blk[10:11, HGW:D] = dqnw8[...]
        blk[11:12, 0:128] = dknw2[...]
        blk[12:14, 0:HGW] = dlg0[0]
        blk[14:16, 0:HGW] = dlg1[0]
        blk[16:24, 0:128] = dsk[...]
        blk[24:25, :] = sq_ref[...]
        _gather_blocks([blk], [out_ref], send_sems, recv_sems, local_sems)
        acc = out_ref[0]
        for i in range(1, 8):
            acc = acc + out_ref[i]
        tot_ref[...] = acc

    vm = pl.BlockSpec(memory_space=pltpu.VMEM)
    return pl.pallas_call(
        body, name="ag_small",
        out_shape=[jax.ShapeDtypeStruct((8, 32, D), F32), jax.ShapeDtypeStruct((32, D), F32)],
        in_specs=[vm] * 13, out_specs=[vm, vm],
        scratch_shapes=[pltpu.VMEM((32, D), F32)] + _gather_sems(1),
        compiler_params=pltpu.CompilerParams(collective_id=BARRIER_IDS["both"]))(*raw, sq)


def _pairx_shapes(units):
    return [jax.ShapeDtypeStruct((4,) + g.shape[2:], g.dtype) for g in units]


def _pairx_phases(g_refs, r_refs, send_sems, recv_sems):
    n = len(g_refs)
    x, y, c = _place()
    cps = [pltpu.make_async_remote_copy(
        src_ref=g_refs[u].at[j, 1 - c], dst_ref=r_refs[u].at[j], send_sem=send_sems.at[u, j],
        recv_sem=recv_sems.at[u, j], device_id=(x, y, 1 - c), device_id_type=MESH)
        for u in range(n) for j in range(4)]

    def start():
        for cp in cps:
            cp.start()

    def end():
        for cp in cps:
            cp.wait()

    return start, None, end


def _carry_pairx(units):
    n = len(units)
    return _Carry(units, _pairx_shapes(units), {},
                  [pltpu.SemaphoreType.DMA((n, 4)), pltpu.SemaphoreType.DMA((n, 4))],
                  lambda ins, outs, sems: _pairx_phases(ins, outs, *sems), "sib")


def _pair_exchange(units):
    n = len(units)

    def body(*refs):
        _peer_barrier("sib")
        start, _, end = _pairx_phases(refs[:n], refs[n:2 * n], *refs[2 * n:])
        start()
        end()

    return pl.pallas_call(
        body, name="pair_exchange", out_shape=_pairx_shapes(units), in_specs=[ANY] * n, out_specs=[ANY] * n,
        scratch_shapes=[pltpu.SemaphoreType.DMA((n, 4))] * 2,
        compiler_params=pltpu.CompilerParams(collective_id=BARRIER_IDS["sib"]))(*units)


IN_ROWS_WITH_DH = 0.6


def _split_rows(h, share):
    first = int(h * share) // BF16_SUBLANES * BF16_SUBLANES
    return (0, first), (first, h - first)


def _rs_pair_add(units, recvs, c):
    n = len(units)

    def body(c_ref, *refs):
        for u in range(n):
            refs[2 * n + u][...] = (refs[u][0].astype(F32) + refs[n + u][...].astype(F32)).astype(BF16)

    in_specs, out_specs, out_shape = [], [], []
    for g in units:
        h, w = g.shape[2:]
        in_specs.append(pl.BlockSpec((1, 1, h, w), lambda j, cr: (j, cr[0], 0, 0)))
    for g in units:
        h, w = g.shape[2:]
        in_specs.append(pl.BlockSpec((1, h, w), lambda j, cr: (j, 0, 0)))
        out_specs.append(pl.BlockSpec((1, h, w), lambda j, cr: (j, 0, 0)))
        out_shape.append(jax.ShapeDtypeStruct((4, h, w), BF16))
    return pl.pallas_call(
        body, name="rs_pair_add",
        grid_spec=pltpu.PrefetchScalarGridSpec(num_scalar_prefetch=1, grid=(4,), in_specs=in_specs,
                                               out_specs=out_specs),
        out_shape=_out_hbm(out_shape),
        compiler_params=pltpu.CompilerParams(vmem_limit_bytes=48 << 20))(
            c.reshape(1), *_in_hbm(list(units) + list(recvs)))


def _chipx_phases(p_refs, r_refs, send_sems, recv_sems, rows=None):
    n = len(p_refs)
    x, y, c = _place()
    k = 2 * x + y

    def part(ref):
        return ref if rows is None else ref.at[pl.ds(rows[0], rows[1])]

    sends = []
    for d in range(1, 4):
        j = (k + d) % 4
        for u in range(n):
            sends.append(pltpu.make_async_remote_copy(
                src_ref=part(p_refs[u].at[j]), dst_ref=part(r_refs[u].at[k]), send_sem=send_sems.at[u, d - 1],
                recv_sem=recv_sems.at[u, d - 1], device_id=(j // 2, j % 2, c), device_id_type=MESH))

    def start():
        for cp in sends:
            cp.start()

    def end():
        for d in range(1, 4):
            src = (k + 4 - d) % 4
            for u in range(n):
                pltpu.make_async_remote_copy(
                    src_ref=part(p_refs[u].at[src]), dst_ref=part(r_refs[u].at[src]),
                    send_sem=send_sems.at[u, d - 1], recv_sem=recv_sems.at[u, d - 1], device_id=(x, y, c),
                    device_id_type=MESH).wait_recv()
        for cp in sends:
            cp.wait_send()

    return start, None, end


def _carry_chipx(pairs, rows=None, into=None):
    n = len(pairs)
    sems = [pltpu.SemaphoreType.DMA((n, 3)), pltpu.SemaphoreType.DMA((n, 3))]
    shapes = [jax.ShapeDtypeStruct(p.shape, p.dtype) for p in pairs]
    if into is None:
        return _Carry(pairs, shapes, {}, sems, lambda ins, outs, s: _chipx_phases(ins, outs, *s, rows=rows),
                      "chips")
    return _Carry(list(pairs) + list(into), shapes, {n + u: u for u in range(n)}, sems,
                  lambda ins, outs, s: _chipx_phases(ins[:n], outs, *s, rows=rows), "chips")


def _rs_chip_add(pairs, contribs, c, chip):
    n = len(pairs)

    def body(s_ref, *refs):
        for u in range(n):
            a, b, c_, d = refs[4 * u:4 * u + 4]
            refs[4 * n + u][0] = ((a[0].astype(F32) + b[0].astype(F32)) + c_[0].astype(F32)) + d[0].astype(F32)

    in_specs, out_specs, out_shape, args = [], [], [], []
    for p, r in zip(pairs, contribs):
        h, w = p.shape[1] // 2, p.shape[2]
        in_specs += [pl.BlockSpec((1, h, w), functools.partial(lambda d, i, s: ((s[1] + d) % 4, i, 0), d))
                     for d in range(4)]
        args += [p, r, r, r]
        out_specs.append(pl.BlockSpec((1, h, w), lambda i, s: (s[0], i, 0)))
        out_shape.append(jax.ShapeDtypeStruct((2, 2 * h, w), F32))
    return pl.pallas_call(
        body, name="rs_chip_add",
        grid_spec=pltpu.PrefetchScalarGridSpec(num_scalar_prefetch=1, grid=(2,), in_specs=in_specs,
                                               out_specs=out_specs),
        out_shape=_out_hbm(out_shape),
        compiler_params=pltpu.CompilerParams(vmem_limit_bytes=48 << 20))(jnp.stack([c, chip]), *_in_hbm(args))


def _rs_sibling_gather(reds):
    n = len(reds)

    def body(*refs):
        _peer_barrier("sib")
        start, _, end = _sibx_phases(refs[n:2 * n], *refs[2 * n:])
        start()
        end()

    return pl.pallas_call(
        body, name="rs_sibling_gather", out_shape=[jax.ShapeDtypeStruct(r.shape, r.dtype) for r in reds],
        in_specs=[ANY] * n, out_specs=[ANY] * n, input_output_aliases={u: u for u in range(n)},
        scratch_shapes=[pltpu.SemaphoreType.DMA((n,))] * 2,
        compiler_params=pltpu.CompilerParams(collective_id=BARRIER_IDS["sib"]))(*reds)


def _sibx_phases(o_refs, send_sems, recv_sems):
    n = len(o_refs)
    x, y, c = _place()
    cps = [pltpu.make_async_remote_copy(
        src_ref=o_refs[u].at[c], dst_ref=o_refs[u].at[c], send_sem=send_sems.at[u], recv_sem=recv_sems.at[u],
        device_id=(x, y, 1 - c), device_id_type=MESH) for u in range(n)]

    def start():
        for cp in cps:
            cp.start()

    def end():
        for u in range(n):
            cps[u].wait_send()
            pltpu.make_async_remote_copy(
                src_ref=o_refs[u].at[1 - c], dst_ref=o_refs[u].at[1 - c], send_sem=send_sems.at[u],
                recv_sem=recv_sems.at[u], device_id=(x, y, 1 - c), device_id_type=MESH).wait_recv()

    return start, None, end


def _carry_sibx(reds):
    n = len(reds)
    return _Carry(reds, [jax.ShapeDtypeStruct(r.shape, r.dtype) for r in reds], {u: u for u in range(n)},
                  [pltpu.SemaphoreType.DMA((n,))] * 2, lambda ins, outs, sems: _sibx_phases(outs, *sems), "sib")


def _prologue(blk, c_ctx, w, b, in8):
    n = w.shape[1]

    def body(blk_ref, cctx_ref, w_ref, b_ref, _in_in, g0_ref, c16_ref, g1_ref, in_ref, mod_s,
             s1, r1, l1, s2, r2, l2, s3, r3):
        _peer_barrier("both")
        start, mid, end = _gather_phases([in_ref], s3, r3)
        _gather_blocks([blk_ref], [g0_ref], s1, r1, l1)
        start()
        c16 = jnp.concatenate([g0_ref[i, 0:1, :] for i in range(8)] + [cctx_ref[...], jnp.zeros((7, D), F32)],
                              axis=0)
        c16_ref[...] = c16
        mod_s[...] = _dot(c16 * _sig(c16), w_ref[...], prec=HI) + b_ref[...]
        _gather_blocks([mod_s], [g1_ref], s2, r2, l2)
        mid()
        end()

    vm = pl.BlockSpec(memory_space=pltpu.VMEM)
    return pl.pallas_call(
        body, name="prologue",
        out_shape=[jax.ShapeDtypeStruct((8, 8, D), F32), jax.ShapeDtypeStruct((16, D), F32),
                   jax.ShapeDtypeStruct((8, 16, n), F32), jax.ShapeDtypeStruct(in8.shape, in8.dtype)],
        in_specs=[vm, vm, vm, vm, ANY], out_specs=[vm, vm, vm, ANY], input_output_aliases={4: 3},
        scratch_shapes=[pltpu.VMEM((16, n), F32)] + _gather_sems(1) + _gather_sems(1)
        + [pltpu.SemaphoreType.DMA((1, 7)), pltpu.SemaphoreType.DMA((1, 7))],
        compiler_params=pltpu.CompilerParams(vmem_limit_bytes=48 << 20,
                                             collective_id=BARRIER_IDS["both"]))(blk, c_ctx, w, b, in8)


def _ada_bwd(c16, dmod16, w, carry=None):
    n = w.shape[1]
    tn = 512

    def body(c_ref, d_ref, w_ref, gw_ref, gc_ref):
        j = pl.program_id(0)

        @pl.when(j == 0)
        def _():
            gc_ref[...] = jnp.zeros_like(gc_ref)

        cc = c_ref[...]
        dm = d_ref[...]
        gw_ref[...] = _dot(cc * _sig(cc), dm, TN, prec=HI)
        gc_ref[...] += _dot(dm, w_ref[...], NT, prec=HI)

    return _pcall(body, name="ada_bwd", grid=(n // tn,),
                  in_specs=[_full((16, D)), pl.BlockSpec((16, tn), lambda j: (0, j)),
                            pl.BlockSpec((D, tn), lambda j: (0, j))],
                  out_specs=[pl.BlockSpec((D, tn), lambda j: (0, j)), _full((16, D))],
                  out_shape=[jax.ShapeDtypeStruct((D, n), F32),
                             jax.ShapeDtypeStruct((16, D), F32)], carry=carry)(c16, dmod16, w)


def _adam_math(w, g, m, v):
    c1 = 1.0 - ADAM_B1 ** ADAM_STEP
    c2 = 1.0 - ADAM_B2 ** ADAM_STEP
    nm = ADAM_B1 * m + (1.0 - ADAM_B1) * g
    nv = ADAM_B2 * v + (1.0 - ADAM_B2) * (g * g)
    return -ADAM_LR * ((nm / c1) / (jnp.sqrt(nv / c2) + ADAM_EPS) + ADAM_WD * w), nm, nv


def _adamw_small(ws, gs, ms, vs):
    n = len(ws)

    def body(*refs):
        for u in range(n):
            d_, nm, nv = _adam_math(refs[u][...], refs[n + u][...], refs[2 * n + u][...], refs[3 * n + u][...])
            refs[4 * n + u][...] = d_
            refs[5 * n + u][...] = nm
            refs[6 * n + u][...] = nv

    specs = [_full(w.shape) for w in ws]
    shapes = [jax.ShapeDtypeStruct(w.shape, F32) for w in ws]
    out = _pcall(body, name="adamw_small", grid=(1,), in_specs=specs * 4, out_specs=specs * 3,
                 out_shape=shapes * 3)(*ws, *gs, *ms, *vs)
    return out[:n], out[n:2 * n], out[2 * n:]


def _cctx_grad(parts, c_ctx):
    def body(p_ref, c_ref, o_ref):
        acc = p_ref[0:1, :]
        for k in range(1, 4):
            acc = acc + p_ref[k:k + 1, :]
        cc = c_ref[...]
        s = _sig(cc)
        o_ref[...] = acc * (s * (1.0 + cc * (1.0 - s)))

    return _pcall(body, name="cctx_grad", grid=(1,), in_specs=[_full(parts.shape), _full((1, D))],
                  out_specs=_full((1, D)), out_shape=jax.ShapeDtypeStruct((1, D), F32))(parts, c_ctx)


ADAM_STEPS = 8


def _adamw_multi(ws, gs, ms, vs, *, name, carry=None):
    n = len(ws)

    def body(*refs):
        for u in range(n):
            refs[4 * n + u][...], refs[5 * n + u][...], refs[6 * n + u][...] = _adam_math(
                refs[u][...], refs[n + u][...], refs[2 * n + u][...], refs[3 * n + u][...])

    specs = [pl.BlockSpec((w.shape[0] // ADAM_STEPS, w.shape[1]), lambda i: (i, 0)) for w in ws]
    shapes = [jax.ShapeDtypeStruct(w.shape, F32) for w in ws]
    res = _pcall(body, name=name, grid=(ADAM_STEPS,), in_specs=specs * 4, out_specs=specs * 3,
                 out_shape=shapes * 3, carry=carry)(*ws, *gs, *ms, *vs)
    out, extra = res if carry is not None else (res, None)
    return (out[:n], out[n:2 * n], out[2 * n:]), extra


def kernel(x, c, ctx, c_ctx, w_ada, b_ada, norm_mix_w, norm_ffn_w, w_in, hgrn_lb_logits, hgrn_norm_w, q_norm_w, k_norm_w, attn_sinks, w_branch_hgrn, w_branch_attn, w_out, w_ffn_gate, w_ffn_up, w_ffn_down, loss_target, m_c_ctx, m_w_ada, m_b_ada, m_norm_mix_w, m_norm_ffn_w, m_w_in, m_hgrn_lb_logits, m_hgrn_norm_w, m_q_norm_w, m_k_norm_w, m_attn_sinks, m_w_branch_hgrn, m_w_branch_attn, m_w_out, m_w_ffn_gate, m_w_ffn_up, m_w_ffn_down, v_c_ctx, v_w_ada, v_b_ada, v_norm_mix_w, v_norm_ffn_w, v_w_in, v_hgrn_lb_logits, v_hgrn_norm_w, v_q_norm_w, v_k_norm_w, v_attn_sinks, v_w_branch_hgrn, v_w_branch_attn, v_w_out, v_w_ffn_gate, v_w_ffn_up, v_w_ffn_down):
    xi, yi, ci = _place()
    chip = 2 * xi + yi
    dev = 2 * chip + ci
    s_len = x.shape[1]

    shards = [w_in[0].T, w_branch_hgrn[0], w_branch_attn[0], w_out[0], w_ffn_gate[0].T, w_ffn_up[0].T,
              w_ffn_down[0]]
    bufs = _cast_place(shards, ci, dev)

    lbrow = jnp.pad(hgrn_lb_logits.reshape(1, 512), ((0, 0), (0, D - 512)))
    blk = jnp.concatenate([c, lbrow, jnp.zeros((6, D), F32)], axis=0)
    nada = w_ada.shape[2]
    b_sh = lax.dynamic_slice(b_ada, (0, chip * nada), (1, nada))
    g0, c16, g1, in8 = _prologue(blk, c_ctx[None], w_ada[0], b_sh, bufs[0])
    lg = g0[0::2, 1, :512].reshape(4, 2, 2, 128).transpose(1, 2, 0, 3).reshape(2, 2, HGW)
    modall = g1[0::2].transpose(1, 0, 2).reshape(16, 4 * nada)
    mod = lax.dynamic_slice(modall, (dev, 0), (1, 6 * D)).reshape(6, D)
    modc = modall[8].reshape(6, D)[:2]

    sq, gx, _, small, rs = _local_step(
        x[0], ctx[0], loss_target[0], mod, modc, norm_mix_w, norm_ffn_w, lg, hgrn_norm_w, q_norm_w,
        k_norm_w, attn_sinks[0], in8.reshape(NCOL, D), bufs[1:], dist=(ci, chip))

    def whole(r):
        return r.reshape(2 * r.shape[1], r.shape[2])

    g_dn, g_g, g_u = [whole(r) for r in rs["ffn_done"]]
    g_bh, g_ba, g_o = [whole(r) for r in rs["mix_done"]]
    in_pairs = rs["in_pairs"]

    g2, tot = _ag_small(small["raw"], sq)
    loss = 0.5 * jnp.sum(tot[24]) / D
    dmodc_tot = jnp.pad(tot[6:8].reshape(1, 2 * D), ((0, 0), (0, 4 * D)))
    g_b_ada = tot[0:6].reshape(1, 6 * D) + dmodc_tot
    dmod16 = jnp.concatenate([g2[:, 0:6].reshape(8, 6 * D), dmodc_tot, jnp.zeros((7, 6 * D), F32)], axis=0)
    (g_w_ada, gc_part), in_contribs = _ada_bwd(
        c16, lax.dynamic_slice(dmod16, (0, chip * nada), (16, nada)), w_ada[0],
        carry=_carry_chipx(in_pairs, rows=rs["rest_rows"], into=rs["in_part"]))
    g3, = _allgather([gc_part[8:16]], name="ag_cctx")
    g_c_ctx = _cctx_grad(g3[0::2, 0], c_ctx[None])[0]
    g_nw1 = tot[8:9]
    g_nw2 = tot[9:10]
    g_hw = tot[10, :HGW].reshape(4, HGD).sum(0, keepdims=True)
    g_qnw = tot[10, HGW:].reshape(8, HDIM).sum(0, keepdims=True)
    g_knw = tot[11, :128].reshape(2, HDIM).sum(0, keepdims=True)
    g_sinks = tot[16:24, 0][None]
    g_lg = lax.dynamic_slice(tot[12:16, :HGW].reshape(2, 2, HGW), (0, 0, chip * 128), (2, 2, 128))

    names = ["c_ctx", "w_ada", "b_ada", "norm_mix_w", "norm_ffn_w", "w_in", "hgrn_lb_logits", "hgrn_norm_w",
             "q_norm_w", "k_norm_w", "attn_sinks", "w_branch_hgrn", "w_branch_attn", "w_out", "w_ffn_gate",
             "w_ffn_up", "w_ffn_down"]
    ws = dict(zip(names, [c_ctx, w_ada, b_ada, norm_mix_w, norm_ffn_w, w_in, hgrn_lb_logits, hgrn_norm_w,
                          q_norm_w, k_norm_w, attn_sinks, w_branch_hgrn, w_branch_attn, w_out, w_ffn_gate,
                          w_ffn_up, w_ffn_down]))
    ms = dict(zip(names, [m_c_ctx, m_w_ada, m_b_ada, m_norm_mix_w, m_norm_ffn_w, m_w_in, m_hgrn_lb_logits,
                          m_hgrn_norm_w, m_q_norm_w, m_k_norm_w, m_attn_sinks, m_w_branch_hgrn,
                          m_w_branch_attn, m_w_out, m_w_ffn_gate, m_w_ffn_up, m_w_ffn_down]))
    vs = dict(zip(names, [v_c_ctx, v_w_ada, v_b_ada, v_norm_mix_w, v_norm_ffn_w, v_w_in, v_hgrn_lb_logits,
                          v_hgrn_norm_w, v_q_norm_w, v_k_norm_w, v_attn_sinks, v_w_branch_hgrn,
                          v_w_branch_attn, v_w_out, v_w_ffn_gate, v_w_ffn_up, v_w_ffn_down]))
    transposed = ("w_in", "w_ffn_gate", "w_ffn_up")

    def view(a, n):
        return a[0].T if n in transposed else a[0]

    def unview(a, n):
        return a.T[None] if n in transposed else a[None]

    delta, new_m, new_v, grads = {}, {}, {}, {}

    def big_adamw(group, gs, name, carry=None):
        (d_, m_, v_), extra = _adamw_multi([view(ws[n], n) for n in group], gs, [view(ms[n], n) for n in group],
                                           [view(vs[n], n) for n in group], name=name, carry=carry)
        for i, n in enumerate(group):
            grads[n], delta[n], new_m[n], new_v[n] = (unview(gs[i], n), unview(d_[i], n), unview(m_[i], n),
                                                      unview(v_[i], n))
        return extra

    big_adamw(["w_ffn_down", "w_ffn_gate", "w_ffn_up", "w_out", "w_branch_hgrn", "w_branch_attn"],
              [g_dn, g_g, g_u, g_o, g_bh, g_ba], "adamw_first")
    in_reds = _rs_chip_add(in_pairs, in_contribs, ci, chip)
    g_in, = [whole(r) for r in _rs_sibling_gather(in_reds)]
    big_adamw(["w_in", "w_ada"], [g_in, g_w_ada], "adamw_second")
    grads.update(c_ctx=g_c_ctx, b_ada=g_b_ada, norm_mix_w=g_nw1, norm_ffn_w=g_nw2, hgrn_lb_logits=g_lg,
                 hgrn_norm_w=g_hw, q_norm_w=g_qnw, k_norm_w=g_knw, attn_sinks=g_sinks)
    small_names = [n for n in names if n not in delta]

    def two_d(a):
        return a.reshape(1, -1) if a.ndim == 1 else a

    sd, sm_, sv = _adamw_small(*[[two_d(d[n]) for n in small_names] for d in (ws, grads, ms, vs)])
    for i, n in enumerate(small_names):
        for dst, src in ((delta, sd), (new_m, sm_), (new_v, sv)):
            dst[n] = src[i].reshape(ws[n].shape)
    return (loss, gx[None], *[grads[n] for n in names], *[delta[n] for n in names],
            *[new_m[n] for n in names], *[new_v[n] for n in names])
```

```python
import functools

import numpy as np
import jax
import jax.numpy as jnp
from jax import lax
from jax.experimental import pallas as pl
from jax.experimental.pallas import tpu as pltpu

F32 = jnp.float32
BF16 = jnp.bfloat16
HI = lax.Precision.HIGHEST
MESH = pl.DeviceIdType.MESH

D = 1024
L = 256
TM = 256
HGW = 512
HGD = 128
CH = 32
ATW = 512
HDIM = 64
BLK = 128
GRID_W = 64
DFF = 2816
NCOL = 5376
EPS = 1e-6
ROPE_THETA = 10000.0
BF16_SUBLANES = 16

C_FB, C_INP, C_QHG, C_FF = 0, 1, 2, 3
C_GATES = 1
C_GHG, C_QRAW = 8, 9
C_KV = 20
C_QKV = 6

ADAM_LR, ADAM_B1, ADAM_B2, ADAM_EPS, ADAM_WD, ADAM_STEP = 0.001, 0.9, 0.999, 1e-08, 0.01, 10

NN = (((1,), (0,)), ((), ()))
NT = (((1,), (1,)), ((), ()))
TN = (((0,), (0,)), ((), ()))


def _dot(a, b, dims=NN, prec=None):
    return lax.dot_general(a, b, dims, precision=prec, preferred_element_type=F32)


def _bdot(a, b, dims=NN):
    return _dot(a.astype(BF16), b.astype(BF16), dims)


def _sig(x):
    return 1.0 / (1.0 + jnp.exp(-x))


class _Carry:
    def __init__(self, ins, outs, aliases, scratch, phases, peers):
        self.ins, self.outs, self.aliases, self.scratch, self.phases = ins, outs, aliases, scratch, phases
        self.peers = peers


BARRIER_IDS = {"sib": 1, "chips": 2, "both": 3}


def _peer_barrier(kind):
    x, y, c = _place()
    peers = []
    if kind in ("sib", "both"):
        peers.append((x, y, 1 - c))
    if kind in ("chips", "both"):
        peers += [(1 - x, y, c), (x, 1 - y, c), (1 - x, 1 - y, c)]
    bar = pltpu.get_barrier_semaphore()
    for peer in peers:
        pl.semaphore_signal(bar, inc=1, device_id=peer, device_id_type=MESH)
    pl.semaphore_wait(bar, len(peers))


def _in_hbm(args):
    return [pltpu.with_memory_space_constraint(a, pltpu.HBM) for a in args]


def _out_hbm(shapes):
    if isinstance(shapes, (list, tuple)):
        return [pltpu.HBM(s.shape, s.dtype) for s in shapes]
    return pltpu.HBM(shapes.shape, shapes.dtype)


def _carry_join(a, b):
    na_in, na_out, na_sc = len(a.ins), len(a.outs), len(a.scratch)
    aliases = dict(a.aliases)
    aliases.update({na_in + i: na_out + o for i, o in b.aliases.items()})

    def phases(ins, outs, sems):
        pa = a.phases(ins[:na_in], outs[:na_out], sems[:na_sc])
        pb = b.phases(ins[na_in:], outs[na_out:], sems[na_sc:])

        def both(fa, fb):
            if fa is None and fb is None:
                return None

            def run():
                for fn in (fa, fb):
                    if fn is not None:
                        fn()
            return run

        return tuple(both(fa, fb) for fa, fb in zip(pa, pb))

    return _Carry(list(a.ins) + list(b.ins), list(a.outs) + list(b.outs), aliases,
                  list(a.scratch) + list(b.scratch), phases, a.peers if a.peers == b.peers else "both")


def _pcall(body, *, name, grid, in_specs, out_specs, out_shape, scratch=(), aliases=None, vmem_mb=48,
           carry=None):
    params = pltpu.CompilerParams(dimension_semantics=("arbitrary",) * len(grid),
                                  vmem_limit_bytes=vmem_mb << 20)
    if carry is None:
        plain = pl.pallas_call(
            body, name=name, grid=grid, in_specs=in_specs, out_specs=out_specs, out_shape=_out_hbm(out_shape),
            scratch_shapes=list(scratch), input_output_aliases=aliases or {}, compiler_params=params)
        return lambda *args: plain(*_in_hbm(args))
    single = not isinstance(out_shape, (list, tuple))
    out_specs_l = [out_specs] if single else list(out_specs)
    out_shape_l = [out_shape] if single else list(out_shape)
    n_in, n_out, n_sc = len(in_specs), len(out_shape_l), len(scratch)
    k_in, k_out = len(carry.ins), len(carry.outs)
    nsteps = int(np.prod(grid))
    assert nsteps >= 3

    def wrapped(*refs):
        ins, cins = refs[:n_in], refs[n_in:n_in + k_in]
        o0 = n_in + k_in
        outs, couts = refs[o0:o0 + n_out], refs[o0 + n_out:o0 + n_out + k_out]
        s0 = o0 + n_out + k_out
        sc, csc = refs[s0:s0 + n_sc], refs[s0 + n_sc:]
        step = pl.program_id(0)
        for ax in range(1, len(grid)):
            step = step * grid[ax] + pl.program_id(ax)
        start, mid, end = carry.phases(cins, couts, csc)

        @pl.when(step == 0)
        def _():
            _peer_barrier(carry.peers)
            start()

        body(*ins, *outs, *sc)
        if mid is not None:
            pl.when(step == nsteps - 2)(mid)
        pl.when(step == nsteps - 1)(end)

    all_aliases = dict(aliases or {})
    all_aliases.update({n_in + i: n_out + o for i, o in carry.aliases.items()})
    call = pl.pallas_call(
        wrapped, name=name, grid=grid, in_specs=list(in_specs) + [ANY] * k_in,
        out_specs=out_specs_l + [ANY] * k_out, out_shape=_out_hbm(out_shape_l + list(carry.outs)),
        scratch_shapes=list(scratch) + list(carry.scratch), input_output_aliases=all_aliases,
        compiler_params=pltpu.CompilerParams(dimension_semantics=("arbitrary",) * len(grid),
                                             vmem_limit_bytes=vmem_mb << 20,
                                             collective_id=BARRIER_IDS[carry.peers]))

    def run(*args):
        res = call(*_in_hbm(args), *carry.ins)
        core = res[:n_out]
        return (core[0] if single else list(core)), list(res[n_out:])

    return run


def _full(shape):
    nd = len(shape)
    return pl.BlockSpec(shape, lambda *_: (0,) * nd)


ANY = pl.BlockSpec(memory_space=pl.ANY)


NT_IN = NCOL // 256


def _src_block(j):
    return j + jnp.where(j < 4, 2, jnp.where(j < 6, 3, jnp.where(j < 8, -6, jnp.where(
        j < 16, 5, jnp.where(j < 20, -7, -14)))))


def _mm_in(h, wt, tm, carry=None):
    tt = h.shape[0]

    def body(h_ref, w_ref, o_ref):
        o_ref[...] = _bdot(h_ref[...], w_ref[...], NT)

    return _pcall(body, name="mm_in", grid=(tt // tm, NT_IN),
                  in_specs=[pl.BlockSpec((tm, D), lambda i, j: (i, 0)),
                            pl.BlockSpec((256, D), lambda i, j: (_src_block(j), 0))],
                  out_specs=pl.BlockSpec((tm, 256), lambda i, j: (i, j)),
                  out_shape=jax.ShapeDtypeStruct((tt, NCOL), F32), carry=carry)(h, wt)


def _mm_dh(dp, wt, tm, carry=None):
    tt = dp.shape[0]
    per, ng = 3, NT_IN // 3

    def body(d_ref, w0, w1, w2, o_ref, acc):
        kk = pl.program_id(1)

        @pl.when(kk == 0)
        def _():
            acc[...] = jnp.zeros_like(acc)

        acc[...] += (_bdot(d_ref[:, 0:256], w0[...]) + _bdot(d_ref[:, 256:512], w1[...])
                     + _bdot(d_ref[:, 512:768], w2[...]))

        @pl.when(kk == ng - 1)
        def _():
            o_ref[...] = acc[...]

    wspecs = [pl.BlockSpec((256, D), functools.partial(lambda t, i, kk: (_src_block(per * kk + t), 0), t))
              for t in range(per)]
    return _pcall(body, name="mm_dh", grid=(tt // tm, ng),
                  in_specs=[pl.BlockSpec((tm, per * 256), lambda i, kk: (i, kk))] + wspecs,
                  out_specs=pl.BlockSpec((tm, D), lambda i, kk: (i, 0)),
                  out_shape=jax.ShapeDtypeStruct((tt, D), F32), scratch=[pltpu.VMEM((tm, D), F32)],
                  carry=carry)(dp, wt, wt, wt)


def _mm_gin(dp, h, tk):
    tt = dp.shape[0]
    nk = tt // tk

    def body(d_ref, h_ref, o_ref, acc):
        kk = pl.program_id(1)

        @pl.when(kk == 0)
        def _():
            acc[...] = jnp.zeros_like(acc)

        acc[...] += _bdot(d_ref[...], h_ref[...], TN)

        @pl.when(kk == nk - 1)
        def _():
            o_ref[...] = acc[...].astype(BF16)

    return _pcall(body, name="mm_gin", grid=(NT_IN, nk),
                  in_specs=[pl.BlockSpec((tk, 256), lambda j, kk: (kk, j)),
                            pl.BlockSpec((tk, D), lambda j, kk: (kk, 0))],
                  out_specs=pl.BlockSpec((256, D), lambda j, kk: (_src_block(j), 0)),
                  out_shape=jax.ShapeDtypeStruct((NCOL, D), BF16), scratch=[pltpu.VMEM((256, D), F32)])(dp, h)


def _tok_specs():
    assert L == TM
    return [_full((TM, D)), pl.BlockSpec((TM, D), lambda i: (jnp.maximum(i - 1, 0), 0))]


def _mod1(ctx, x, nw, ss):
    rows = L + x.shape[0]

    def body(c_ref, x_ref, nw_ref, ss_ref, h_ref):
        t = jnp.where(pl.program_id(0) == 0, c_ref[...], x_ref[...])
        r = lax.rsqrt(jnp.mean(t * t, axis=-1, keepdims=True) + EPS)
        s = ss_ref[0]
        h_ref[...] = ((t * r * nw_ref[...]) * (1.0 + s[1:2]) + s[0:1]).astype(BF16)

    return _pcall(body, name="mod1", grid=(rows // TM,),
                  in_specs=_tok_specs() + [_full((1, D)),
                                           pl.BlockSpec((1, 2, D), lambda i: (jnp.minimum(i, 1), 0, 0))],
                  out_specs=pl.BlockSpec((TM, D), lambda i: (i, 0)),
                  out_shape=jax.ShapeDtypeStruct((rows, D), BF16))(ctx, x, nw, ss)


def _norm_bwd_rows(x, dh, nw, scale):
    r = lax.rsqrt(jnp.mean(x * x, axis=-1, keepdims=True) + EPS)
    xh = x * r
    dxh = dh * ((1.0 + scale) * nw)
    dx = r * (dxh - xh * jnp.mean(dxh * xh, axis=-1, keepdims=True))
    return dx, xh


def _out_proj_mod2(mixed, w_o, x, g1, nw2, ss2):
    s_len = x.shape[0]
    tm = 512

    def body(m_ref, w_ref, x_ref, g_ref, nw_ref, ss_ref, ao_ref, x1_ref, h_ref):
        ao = _bdot(m_ref[...], w_ref[...])
        ao_ref[...] = ao.astype(BF16)
        x1 = x_ref[...] + g_ref[...] * ao
        x1_ref[...] = x1
        r = lax.rsqrt(jnp.mean(x1 * x1, axis=-1, keepdims=True) + EPS)
        s = ss_ref[0]
        h_ref[...] = ((x1 * r * nw_ref[...]) * (1.0 + s[1:2]) + s[0:1]).astype(BF16)

    row = pl.BlockSpec((tm, D), lambda i: (i, 0))
    f = jax.ShapeDtypeStruct((s_len, D), F32)
    return _pcall(body, name="out_proj_mod2", grid=(s_len // tm,),
                  in_specs=[row, _full((D, D)), row, _full((1, D)), _full((1, D)), _full((1, 2, D))],
                  out_specs=[row, row, row],
                  out_shape=[jax.ShapeDtypeStruct((s_len, D), BF16), f,
                             jax.ShapeDtypeStruct((s_len, D), BF16)])(mixed, w_o, x, g1, nw2, ss2)


TS = 1024


def _acc_call(body, *, name, grid, in_specs, out_specs, out_shape, acc_shapes, args, carry=None):
    return _pcall(body, name=name, grid=grid, in_specs=in_specs, out_specs=out_specs, out_shape=out_shape,
                  scratch=[pltpu.VMEM(s, F32) for s in acc_shapes], carry=carry)(*args)


def _ffn_up(h2, g4, u4, carry=None):
    s_len = h2.shape[0]
    ns = g4.shape[1]

    def body(h_ref, g_ref, u_ref, a_ref, b_ref, z_ref):
        h = h_ref[...]
        a = _bdot(h, g_ref[0], NT)
        b = _bdot(h, u_ref[0], NT)
        a_ref[0] = a.astype(BF16)
        b_ref[0] = b.astype(BF16)
        z_ref[0] = (a * _sig(a) * b).astype(BF16)

    w = pl.BlockSpec((1, ns, D), lambda i, j: (j, 0, 0))
    o = pl.BlockSpec((1, TS, ns), lambda i, j: (j, i, 0))
    f = jax.ShapeDtypeStruct((4, s_len, ns), BF16)
    return _pcall(body, name="ffn_up", grid=(s_len // TS, 4),
                  in_specs=[pl.BlockSpec((TS, D), lambda i, j: (i, 0)), w, w], out_specs=[o, o, o],
                  out_shape=[f, f, jax.ShapeDtypeStruct((4, s_len, ns), BF16)], carry=carry)(h2, g4, u4)


def _ffn_down_loss(z4, dn4, x1, g2, tgt):
    _, s_len, ns = z4.shape

    def body(z_ref, w_ref, x1_ref, g_ref, t_ref, sq_ref, dx2_ref, dyb_ref, dg_ref, acc):
        i, j = pl.program_id(0), pl.program_id(1)

        @pl.when((i == 0) & (j == 0))
        def _():
            sq_ref[...] = jnp.zeros_like(sq_ref)
            dg_ref[...] = jnp.zeros_like(dg_ref)

        @pl.when(j == 0)
        def _():
            acc[...] = jnp.zeros_like(acc)

        acc[...] += _bdot(z_ref[0], w_ref[0])

        @pl.when(j == 3)
        def _():
            y_ = acc[...]
            g = g_ref[...]
            e = x1_ref[...] + g * y_ - t_ref[...]
            sq_ref[...] += jnp.sum(e * e, axis=0, keepdims=True)
            dx2 = e * (1.0 / D)
            dx2_ref[...] = dx2
            dyb_ref[...] = (g * dx2).astype(BF16)
            dg_ref[...] += jnp.sum(dx2 * y_, axis=0, keepdims=True)

    row = pl.BlockSpec((TS, D), lambda i, j: (i, 0))
    vec = _full((1, D))
    return _acc_call(body, name="ffn_down_loss", grid=(s_len // TS, 4),
                     in_specs=[pl.BlockSpec((1, TS, ns), lambda i, j: (j, i, 0)),
                               pl.BlockSpec((1, ns, D), lambda i, j: (j, 0, 0)), row, vec, row],
                     out_specs=[vec, row, row, vec],
                     out_shape=[jax.ShapeDtypeStruct((1, D), F32), jax.ShapeDtypeStruct((s_len, D), F32),
                                jax.ShapeDtypeStruct((s_len, D), BF16), jax.ShapeDtypeStruct((1, D), F32)],
                     acc_shapes=[(TS, D)], args=(z4, dn4, x1, g2, tgt))


def _ffn_dz(dyb, dn4, a4, b4):
    _, s_len, ns = a4.shape

    def body(dy_ref, w_ref, a_ref, b_ref, da_ref, db_ref):
        dz = _bdot(dy_ref[...], w_ref[0], NT)
        a = a_ref[0].astype(F32)
        s = _sig(a)
        da_ref[0] = (dz * b_ref[0].astype(F32) * (s * (1.0 + a * (1.0 - s)))).astype(BF16)
        db_ref[0] = (dz * (a * s)).astype(BF16)

    t = pl.BlockSpec((1, TS, ns), lambda i, j: (j, i, 0))
    o = jax.ShapeDtypeStruct((4, s_len, ns), BF16)
    return _pcall(body, name="ffn_dz", grid=(s_len // TS, 4),
                  in_specs=[pl.BlockSpec((TS, D), lambda i, j: (i, 0)),
                            pl.BlockSpec((1, ns, D), lambda i, j: (j, 0, 0)), t, t],
                  out_specs=[t, t], out_shape=[o, o])(dyb, dn4, a4, b4)


def _ffn_gdn(z4, dyb):
    _, s_len, ns = z4.shape
    tk = min(s_len, 2 * TS)
    nk = s_len // tk

    def body(z_ref, dy_ref, o_ref, acc):
        t = pl.program_id(1)

        @pl.when(t == 0)
        def _():
            acc[...] = jnp.zeros_like(acc)

        acc[...] += _bdot(z_ref[0], dy_ref[...], TN)

        @pl.when(t == nk - 1)
        def _():
            o_ref[0] = acc[...].astype(o_ref.dtype)

    return _acc_call(body, name="ffn_gdn", grid=(4, nk),
                     in_specs=[pl.BlockSpec((1, tk, ns), lambda j, t: (j, t, 0)),
                               pl.BlockSpec((tk, D), lambda j, t: (t, 0))],
                     out_specs=pl.BlockSpec((1, ns, D), lambda j, t: (j, 0, 0)),
                     out_shape=jax.ShapeDtypeStruct((4, ns, D), BF16), acc_shapes=[(ns, D)], args=(z4, dyb))


def _ffn_dh2(da4, db4, g4, u4, carry=None):
    _, s_len, ns = da4.shape

    def body(da_ref, db_ref, g_ref, u_ref, o_ref, acc):
        j = pl.program_id(1)

        @pl.when(j == 0)
        def _():
            acc[...] = jnp.zeros_like(acc)

        acc[...] += _bdot(da_ref[0], g_ref[0]) + _bdot(db_ref[0], u_ref[0])

        @pl.when(j == 3)
        def _():
            o_ref[...] = acc[...]

    t = pl.BlockSpec((1, TS, ns), lambda i, j: (j, i, 0))
    w = pl.BlockSpec((1, ns, D), lambda i, j: (j, 0, 0))
    return _acc_call(body, name="ffn_dh2", grid=(s_len // TS, 4), in_specs=[t, t, w, w],
                     out_specs=pl.BlockSpec((TS, D), lambda i, j: (i, 0)),
                     out_shape=jax.ShapeDtypeStruct((s_len, D), F32), acc_shapes=[(TS, D)],
                     args=(da4, db4, g4, u4), carry=carry)


def _ffn_ggu(h2, da4, db4, carry=None):
    _, s_len, ns = da4.shape
    nk = s_len // TS

    def body(h_ref, da_ref, db_ref, gg_ref, gu_ref, acc_g, acc_u):
        t = pl.program_id(1)

        @pl.when(t == 0)
        def _():
            acc_g[...] = jnp.zeros_like(acc_g)
            acc_u[...] = jnp.zeros_like(acc_u)

        h = h_ref[...]
        acc_g[...] += _bdot(da_ref[0], h, TN)
        acc_u[...] += _bdot(db_ref[0], h, TN)

        @pl.when(t == nk - 1)
        def _():
            gg_ref[0] = acc_g[...].astype(BF16)
            gu_ref[0] = acc_u[...].astype(BF16)

    d = pl.BlockSpec((1, TS, ns), lambda j, t: (j, t, 0))
    o = pl.BlockSpec((1, ns, D), lambda j, t: (j, 0, 0))
    f = jax.ShapeDtypeStruct((4, ns, D), BF16)
    return _acc_call(body, name="ffn_ggu", grid=(4, nk),
                     in_specs=[pl.BlockSpec((TS, D), lambda j, t: (t, 0)), d, d], out_specs=[o, o],
                     out_shape=[f, f], acc_shapes=[(ns, D), (ns, D)], args=(h2, da4, db4), carry=carry)


def _mod2_bwd(x1, dh2, dx2, ao, nw2, ss2, g1):
    s_len = x1.shape[0]

    def body(x1_ref, dh_ref, dx2_ref, ao_ref, nw_ref, ss_ref, g_ref,
             dx1_ref, da_ref, dss_ref, dnw_ref, dg_ref):
        i = pl.program_id(0)

        @pl.when(i == 0)
        def _():
            dss_ref[...] = jnp.zeros_like(dss_ref)
            dnw_ref[...] = jnp.zeros_like(dnw_ref)
            dg_ref[...] = jnp.zeros_like(dg_ref)

        dh = dh_ref[...]
        nw = nw_ref[...]
        scale = ss_ref[0][1:2]
        dxn, xh = _norm_bwd_rows(x1_ref[...], dh, nw, scale)
        dx1 = dx2_ref[...] + dxn
        dx1_ref[...] = dx1
        da_ref[...] = (g_ref[...] * dx1).astype(BF16)
        dg_ref[...] += jnp.sum(dx1 * ao_ref[...].astype(F32), axis=0, keepdims=True)
        dsh = jnp.sum(dh, axis=0, keepdims=True)
        dsc = jnp.sum(dh * xh * nw, axis=0, keepdims=True)
        dss_ref[...] += jnp.concatenate([dsh, dsc], axis=0)
        dnw_ref[...] += jnp.sum(dh * xh * (1.0 + scale), axis=0, keepdims=True)

    row = pl.BlockSpec((TM, D), lambda i: (i, 0))
    vec = _full((1, D))
    return _pcall(body, name="mod2_bwd", grid=(s_len // TM,),
                  in_specs=[row, row, row, row, vec, _full((1, 2, D)), vec],
                  out_specs=[row, row, _full((2, D)), vec, vec],
                  out_shape=[jax.ShapeDtypeStruct((s_len, D), F32), jax.ShapeDtypeStruct((s_len, D), BF16),
                             jax.ShapeDtypeStruct((2, D), F32), jax.ShapeDtypeStruct((1, D), F32),
                             jax.ShapeDtypeStruct((1, D), F32)])(x1, dh2, dx2, ao, nw2, ss2, g1)


def _mod1_bwd(ctx, x, dh, dx1, nw1, ss1):
    s_len = dx1.shape[0]
    tt = L + s_len

    def body(c_ref, x_ref, dh_ref, dx1_ref, nw_ref, ss_ref, dx_ref, dss_ref, dnw_ref):
        i = pl.program_id(0)
        tok = jnp.where(i == 0, c_ref[...], x_ref[...])

        @pl.when(i == 0)
        def _():
            dnw_ref[...] = jnp.zeros_like(dnw_ref)

        @pl.when(i <= 1)
        def _():
            dss_ref[...] = jnp.zeros_like(dss_ref)

        dh_ = dh_ref[...]
        nw = nw_ref[...]
        scale = ss_ref[0][1:2]
        dxn, xh = _norm_bwd_rows(tok, dh_, nw, scale)

        @pl.when(i >= 1)
        def _():
            dx_ref[...] = dx1_ref[...] + dxn

        dsh = jnp.sum(dh_, axis=0, keepdims=True)
        dsc = jnp.sum(dh_ * xh * nw, axis=0, keepdims=True)
        dss_ref[...] += jnp.concatenate([dsh, dsc], axis=0)[None]
        dnw_ref[...] += jnp.sum(dh_ * xh * (1.0 + scale), axis=0, keepdims=True)

    row = pl.BlockSpec((TM, D), lambda i: (i, 0))
    lat = pl.BlockSpec((TM, D), lambda i: (jnp.maximum(i - 1, 0), 0))
    sel = pl.BlockSpec((1, 2, D), lambda i: (jnp.minimum(i, 1), 0, 0))
    return _pcall(body, name="mod1_bwd", grid=(tt // TM,),
                  in_specs=_tok_specs() + [row, lat, _full((1, D)), sel],
                  out_specs=[lat, sel, _full((1, D))],
                  out_shape=[jax.ShapeDtypeStruct((s_len, D), F32), jax.ShapeDtypeStruct((2, 2, D), F32),
                             jax.ShapeDtypeStruct((1, D), F32)])(ctx, x, dh, dx1, nw1, ss1)


def _rows(c):
    return slice(c * CH, (c + 1) * CH)


def _chunk_masks(rev, transpose=False):
    r = lax.broadcasted_iota(jnp.int32, (TM, TM), 0)
    c = lax.broadcasted_iota(jnp.int32, (TM, TM), 1)
    same = (r // CH) == (c // CH)
    before = (c >= r) if (rev != transpose) else (c <= r)
    return same & before, same


def _chunk_scan(x, rev, transpose=False):
    r = lax.broadcasted_iota(jnp.int32, (CH, CH), 0)
    c = lax.broadcasted_iota(jnp.int32, (CH, CH), 1)
    tri = ((c >= r) if (rev != transpose) else (c <= r)).astype(F32)
    return jnp.concatenate([_dot(tri, x[_rows(ch)], prec=HI) for ch in range(x.shape[0] // CH)], axis=0)


def _chunk_total(x):
    return jnp.concatenate([jnp.broadcast_to(jnp.sum(x[_rows(ch)], axis=0, keepdims=True), (CH, x.shape[1]))
                            for ch in range(x.shape[0] // CH)], axis=0)


def _hgrn_gate(fl, qraw, lg):
    lb = 1.0 / (1.0 + jnp.exp(lg[1:2] - lg[0:1]))
    sg = _sig(fl)
    f = lb + (1.0 - lb) * sg
    q = qraw * _sig(qraw) * (HGD ** -0.5)
    return lb, sg, f, q


def _hgrn_fwd(p, lg, *, rev, carry=None, readout=None):
    tt = p.shape[0]
    nt = tt // TM
    ncht = TM // CH
    d = 1 if rev else 0

    def tile_of(s):
        return jnp.where(s == 0, 0, nt - s) if rev else s

    def body(*refs):
        if readout is None:
            f_ref, inp_ref, q_ref, lg_ref, o_ref, st_ref, state = refs
        else:
            f_ref, inp_ref, q_ref, lg_ref, oo_ref, g_ref, hw_ref, o_ref, st_ref, y_ref, state = refs
        s = pl.program_id(0)

        @pl.when(s == 0)
        def _():
            state[...] = jnp.zeros_like(state)

        _, _, f, q = _hgrn_gate(f_ref[...], q_ref[...], lg_ref[0])
        lf = jnp.log(f)
        causal, _ = _chunk_masks(rev)
        cum = _chunk_scan(lf, rev)
        tot = _chunk_total(lf)
        qd = (q * jnp.exp(cum)).astype(BF16)
        kd = ((1.0 - f) * jnp.exp(-cum)).astype(BF16)
        ke = ((1.0 - f) * jnp.exp(tot - cum)).astype(BF16)
        et = jnp.exp(tot)
        v = inp_ref[...].astype(BF16)
        order = range(ncht - 1, -1, -1) if rev else range(ncht)
        outs = []
        for h in range(4):
            sl = slice(h * HGD, (h + 1) * HGD)
            qd_, kd_, ke_, v_ = qd[:, sl], kd[:, sl], ke[:, sl], v[:, sl]
            pm = jnp.where(causal, _dot(qd_, kd_, NT), 0.0).astype(BF16)
            o_h = _dot(pm, v_)
            upd = [_dot(v_[_rows(c)], ke_[_rows(c)], TN) for c in range(ncht)]
            st = state[h]
            for c in order:
                st_ref[c, h] = st
                st = st * et[c * CH:c * CH + 1, sl] + upd[c]
            state[h] = st
            inter = [_dot(qd_[_rows(c)], st_ref[c, h].astype(BF16), NT) for c in range(ncht)]
            outs.append(o_h + jnp.concatenate(inter, axis=0))
        o_tile = jnp.concatenate(outs, axis=1)
        o_ref[...] = o_tile
        if readout is not None:
            @pl.when(tile_of(s) >= 1)
            def _():
                g = g_ref[...]
                y_ref[...] = (_head_rms(oo_ref[...] + o_tile, None, 4) * hw_ref[...] * (g * _sig(g))).astype(BF16)

    def col(cb):
        return pl.BlockSpec((TM, HGW), lambda s: (tile_of(s), cb))

    in_specs = [col(C_FB if rev else C_FF), col(C_INP), col(C_QHG), pl.BlockSpec((1, 2, HGW), lambda s: (d, 0, 0))]
    out_specs = [col(0), pl.BlockSpec((ncht, 4, HGD, HGD), lambda s: (tile_of(s), 0, 0, 0))]
    out_shape = [jax.ShapeDtypeStruct((tt, HGW), F32), jax.ShapeDtypeStruct((nt * ncht, 4, HGD, HGD), F32)]
    args = [p, p, p, lg]
    if readout is not None:
        in_specs += [col(0), col(C_GHG), _full((1, HGW))]
        args += [readout[0], p, readout[1]]
        assert rev
        out_specs.append(pl.BlockSpec((TM, HGW), lambda s: (jnp.where(s == 0, nt - 2, tile_of(s) - 1), 0)))
        out_shape.append(jax.ShapeDtypeStruct((tt - L, HGW), BF16))
    return _pcall(body, name="hgrn_fwd_rev" if rev else "hgrn_fwd", grid=(nt,), in_specs=in_specs,
                  out_specs=out_specs, out_shape=out_shape, scratch=[pltpu.VMEM((4, HGD, HGD), F32)],
                  carry=carry)(*args)


def _hgrn_bwd(p, lg, do, st, dp, prev, *, rev, carry=None):
    tt = p.shape[0]
    nt = tt // TM
    ncht = TM // CH
    d = 1 if rev else 0
    second = prev is not None

    def tile_of(s):
        return jnp.where(s == nt - 1, 0, s + 1) if rev else nt - 1 - s

    def body(*refs):
        if second:
            (f_ref, inp_ref, q_ref, lg_ref, do_ref, st_ref, dvp_ref, dqp_ref, _dp_in,
             dp_ref, dlg_ref, dstate) = refs
        else:
            (f_ref, inp_ref, q_ref, lg_ref, do_ref, st_ref, _dp_in,
             dp_ref, dv_ref, dq_ref, dlg_ref, dstate) = refs
        s = pl.program_id(0)
        tile = tile_of(s)

        @pl.when(s == 0)
        def _():
            dstate[...] = jnp.zeros_like(dstate)
            dlg_ref[...] = jnp.zeros_like(dlg_ref)

        qraw = q_ref[...]
        lb, sg, f, q = _hgrn_gate(f_ref[...], qraw, lg_ref[0])
        lf = jnp.log(f)
        causal, _ = _chunk_masks(rev)
        causal_t, _ = _chunk_masks(rev, transpose=True)
        cum = _chunk_scan(lf, rev)
        tot = _chunk_total(lf)
        ea, eb, ee, et = jnp.exp(cum), jnp.exp(-cum), jnp.exp(tot - cum), jnp.exp(tot)
        qdf, kdf, kef = q * ea, (1.0 - f) * eb, (1.0 - f) * ee
        qd, kd, ke = qdf.astype(BF16), kdf.astype(BF16), kef.astype(BF16)
        v = inp_ref[...].astype(BF16)
        dob = jnp.where(tile == 0, 0.0, do_ref[...]).astype(BF16)
        order = range(ncht) if rev else range(ncht - 1, -1, -1)
        dq_l, dk_l, dv_l, dcum_l, dtot_l = [], [], [], [], []
        for h in range(4):
            sl = slice(h * HGD, (h + 1) * HGD)
            qd_, kd_, ke_, v_, do_ = qd[:, sl], kd[:, sl], ke[:, sl], v[:, sl], dob[:, sl]
            pmt = jnp.where(causal_t, _dot(kd_, qd_, NT), 0.0).astype(BF16)
            dpm = jnp.where(causal, _dot(do_, v_, NT), 0.0).astype(BF16)
            dpmt = jnp.where(causal_t, _dot(v_, do_, NT), 0.0).astype(BF16)
            dv = _dot(pmt, do_)
            dqd = _dot(dpm, kd_)
            dkd = _dot(dpmt, qd_)
            upd = [_dot(do_[_rows(c)], qd_[_rows(c)], TN) for c in range(ncht)]
            ds = dstate[h]
            ds1 = [None] * ncht
            for c in order:
                ds1[c] = ds
                ds = ds * et[c * CH:c * CH + 1, sl] + upd[c]
            dstate[h] = ds
            dke_c, dv_c, dqd_c, dtot_c = [], [], [], []
            for c in range(ncht):
                st0 = st_ref[c, h]
                dsb = ds1[c].astype(BF16)
                dke_ = _dot(v_[_rows(c)], dsb)
                dke_c.append(dke_)
                dv_c.append(_dot(ke_[_rows(c)], dsb, NT))
                dqd_c.append(_dot(do_[_rows(c)], st0.astype(BF16)))
                dt = (jnp.sum(ds1[c] * st0, axis=0, keepdims=True) * et[c * CH:c * CH + 1, sl]
                      + jnp.sum(dke_ * kef[_rows(c), sl], axis=0, keepdims=True))
                dtot_c.append(jnp.broadcast_to(dt, (CH, HGD)))
            dke = jnp.concatenate(dke_c, axis=0)
            dqd = dqd + jnp.concatenate(dqd_c, axis=0)
            dv_l.append(dv + jnp.concatenate(dv_c, axis=0))
            dtot_l.append(jnp.concatenate(dtot_c, axis=0))
            dq_l.append(dqd * ea[:, sl])
            dk_l.append(dkd * eb[:, sl] + dke * ee[:, sl])
            dcum_l.append(dqd * qdf[:, sl] - dkd * kdf[:, sl] - dke * kef[:, sl])
        dcum = jnp.concatenate(dcum_l, axis=1)
        dlf = _chunk_scan(dcum, rev, transpose=True) + jnp.concatenate(dtot_l, axis=1)
        dq_t = jnp.concatenate(dq_l, axis=1)
        dv_t = jnp.concatenate(dv_l, axis=1)

        df = dlf / f - jnp.concatenate(dk_l, axis=1)
        dfl = df * (1.0 - lb) * sg * (1.0 - sg)
        dlb = jnp.sum(df * (1.0 - sg), axis=0, keepdims=True)
        dl0 = dlb * lb * (1.0 - lb)
        dlg_ref[...] += jnp.concatenate([dl0, -dl0], axis=0)[None]
        if second:
            sq = _sig(qraw)
            dqr = (dqp_ref[...] + dq_t) * (HGD ** -0.5) * (sq * (1.0 + qraw * (1.0 - sq)))
            dp_ref[...] = jnp.concatenate([dfl, dvp_ref[...] + dv_t, dqr], axis=1).astype(BF16)
        else:
            dp_ref[...] = dfl.astype(BF16)
            dv_ref[...] = dv_t
            dq_ref[...] = dq_t

    def col(cb):
        return pl.BlockSpec((TM, HGW), lambda s: (tile_of(s), cb))

    tok = pl.BlockSpec((TM, HGW), lambda s: (tile_of(s), 0))
    in_specs = [col(C_FB if rev else C_FF), col(C_INP), col(C_QHG),
                pl.BlockSpec((1, 2, HGW), lambda s: (d, 0, 0)),
                pl.BlockSpec((TM, HGW), lambda s: (jnp.maximum(tile_of(s) - 1, 0), 0)),
                pl.BlockSpec((ncht, 4, HGD, HGD), lambda s: (tile_of(s), 0, 0, 0))]
    args = [p, p, p, lg, do, st]
    dlg_spec = _full((1, 2, HGW))
    dlg_shape = jax.ShapeDtypeStruct((1, 2, HGW), F32)
    if second:
        in_specs += [tok, tok]
        args += [prev[0], prev[1]]
        out_specs = [pl.BlockSpec((TM, 3 * HGW), lambda s: (tile_of(s), 0)), dlg_spec]
        out_shape = [jax.ShapeDtypeStruct(dp.shape, BF16), dlg_shape]
    else:
        out_specs = [pl.BlockSpec((TM, HGW), lambda s: (tile_of(s), C_FB if rev else C_FF)), tok, tok, dlg_spec]
        out_shape = [jax.ShapeDtypeStruct(dp.shape, BF16), jax.ShapeDtypeStruct((tt, HGW), F32),
                     jax.ShapeDtypeStruct((tt, HGW), F32), dlg_shape]
    in_specs.append(ANY)
    args.append(dp)
    return _pcall(body, name="hgrn_bwd_rev" if rev else "hgrn_bwd", grid=(nt,),
                  in_specs=in_specs, out_specs=out_specs, out_shape=out_shape,
                  scratch=[pltpu.VMEM((4, HGD, HGD), F32)],
                  aliases={len(args) - 1: 0}, carry=carry)(*args)


def _head_rms(o, w, nheads):
    outs = []
    for h in range(nheads):
        oh = o[:, h * HGD:(h + 1) * HGD]
        outs.append(oh * lax.rsqrt(jnp.mean(oh * oh, axis=-1, keepdims=True) + EPS))
    return jnp.concatenate(outs, axis=1)


def _readout_bwd(o0, o1, p, hw4, dy, dp, carry=None):
    tt = o0.shape[0]
    s_len = tt - L

    def body(o0_ref, o1_ref, g_ref, w_ref, dy_ref, _dp_in, dp_ref, do_ref, dw_ref):
        i = pl.program_id(0)

        @pl.when(i == 0)
        def _():
            dw_ref[...] = jnp.zeros_like(dw_ref)
            dp_ref[...] = jnp.zeros_like(dp_ref)

        @pl.when(i >= 1)
        def _():
            o = o0_ref[...] + o1_ref[...]
            g = g_ref[...]
            w = w_ref[...]
            sg = _sig(g)
            dy_ = dy_ref[...]
            dsw = dy_ * (g * sg)
            outs, xhs = [], []
            for h in range(4):
                sl = slice(h * HGD, (h + 1) * HGD)
                oh = o[:, sl]
                r = lax.rsqrt(jnp.mean(oh * oh, axis=-1, keepdims=True) + EPS)
                xh = oh * r
                dxh = dsw[:, sl] * w[:, sl]
                outs.append(r * (dxh - xh * jnp.mean(dxh * xh, axis=-1, keepdims=True)))
                xhs.append(xh)
            xh = jnp.concatenate(xhs, axis=1)
            do_ref[...] = jnp.concatenate(outs, axis=1)
            dp_ref[...] = (dy_ * xh * w * (sg * (1.0 + g * (1.0 - sg)))).astype(BF16)
            dw_ref[...] += jnp.sum(dsw * xh, axis=0, keepdims=True)

    tok = pl.BlockSpec((TM, HGW), lambda i: (i, 0))
    lat = pl.BlockSpec((TM, HGW), lambda i: (jnp.maximum(i - 1, 0), 0))
    return _pcall(body, name="readout_bwd", grid=(tt // TM,),
                  in_specs=[tok, tok, pl.BlockSpec((TM, HGW), lambda i: (i, C_GHG)), _full((1, HGW)), lat, ANY],
                  out_specs=[pl.BlockSpec((TM, HGW), lambda i: (i, C_GHG)), lat, _full((1, HGW))],
                  out_shape=[jax.ShapeDtypeStruct(dp.shape, BF16), jax.ShapeDtypeStruct((s_len, HGW), F32),
                             jax.ShapeDtypeStruct((1, HGW), F32)],
                  aliases={5: 0}, carry=carry)(o0, o1, p, hw4, dy, dp)


def _rope_tables(s_len):
    t = np.arange(s_len)
    inv = ROPE_THETA ** (-np.arange(0, 32, 2, dtype=np.float64) / 32)
    def half(pos):
        ang = pos[:, None].astype(np.float64) * inv[None, :]
        return (np.concatenate([np.cos(ang), np.cos(ang)], 1), np.concatenate([-np.sin(ang), np.sin(ang)], 1))
    cr, sr = half(t // GRID_W)
    cc, sc = half(t % GRID_W)
    cos = np.concatenate([cr, cc, cr, cc], 1)
    sin = np.concatenate([sr, sc, sr, sc], 1)
    cos = np.concatenate([np.ones((L, 128)), cos], 0)
    sin = np.concatenate([np.zeros((L, 128)), sin], 0)
    return jnp.asarray(cos, F32), jnp.asarray(sin, F32)


def _blockdiag(n, w):
    i = np.arange(n)
    return jnp.asarray((i[:, None] // w == i[None, :] // w) / float(w), F32)


def _dup_matrix():
    m = np.zeros((128, 512), np.float32)
    for g in range(2):
        for j in range(4):
            for dd in range(HDIM):
                m[64 * g + dd, 256 * g + 64 * j + dd] = 1.0
    return m


def _head_mean(x, blockdiag):
    return _dot(x, blockdiag, prec=lax.Precision.HIGH)


def _rot(x):
    n = x.shape[1]
    lane = lax.broadcasted_iota(jnp.int32, x.shape, 1)
    return jnp.where((lane % 32) < 16, pltpu.roll(x, n - 16, 1), pltpu.roll(x, 16, 1))


def _qk_prep(p, cos, sin, qnw8, knw2, bd512, bd128, dup):
    tt = p.shape[0]

    def body(q_ref, kv_ref, cos_ref, sin_ref, qw_ref, kw_ref, b5_ref, b1_ref, dup_ref,
             qr_ref, k4_ref, v4_ref):
        cos_, sin_ = cos_ref[...], sin_ref[...]
        q = q_ref[...]
        qn = q * lax.rsqrt(_head_mean(q * q, b5_ref[...]) + EPS) * qw_ref[...]
        cos4 = jnp.concatenate([cos_] * 4, axis=1)
        sin4 = jnp.concatenate([sin_] * 4, axis=1)
        qr_ref[...] = ((qn * cos4 + _rot(qn) * sin4) * (HDIM ** -0.5)).astype(BF16)
        kv = kv_ref[...]
        k, v = kv[:, :128], kv[:, 128:]
        kn = k * lax.rsqrt(_head_mean(k * k, b1_ref[...]) + EPS) * kw_ref[...]
        kr = kn * cos_ + _rot(kn) * sin_
        k4_ref[...] = _bdot(kr, dup_ref[...]).astype(BF16)
        v4_ref[...] = _bdot(v, dup_ref[...]).astype(BF16)

    row = lambda w, cb: pl.BlockSpec((TM, w), lambda i: (i, cb))
    out = jax.ShapeDtypeStruct((tt, ATW), BF16)
    return _pcall(body, name="qk_prep", grid=(tt // TM,),
                  in_specs=[row(ATW, C_QRAW), row(256, C_KV), row(128, 0), row(128, 0),
                            _full((1, ATW)), _full((1, 128)), _full((ATW, ATW)), _full((128, 128)),
                            _full((128, ATW))],
                  out_specs=[row(ATW, 0)] * 3, out_shape=[out] * 3)(
                      p, p, cos, sin, qnw8, knw2, bd512, bd128, dup)


def _attn_masks(i, nb):
    r = lax.broadcasted_iota(jnp.int32, (4 * BLK, 3 * BLK + L), 0) % BLK
    c = lax.broadcasted_iota(jnp.int32, (4 * BLK, 3 * BLK + L), 1)
    kpos = (i - 1) * BLK + c
    loc = (jnp.abs(c - BLK - r) <= BLK) & (kpos >= 0) & (kpos < nb * BLK)
    return loc | (c >= 3 * BLK)


def _stack_mask():
    r = lax.broadcasted_iota(jnp.int32, (4 * BLK, 256), 0)
    lane = lax.broadcasted_iota(jnp.int32, (4 * BLK, 256), 1)
    return (r // BLK) == (lane // HDIM)


def _stack_heads(xg, fill=0.0):
    x4 = jnp.concatenate([xg] * 4, axis=0)
    return jnp.where(_stack_mask(), x4, jnp.full_like(x4, fill))


def _unstack_heads(x4):
    out = jnp.where(_lane_mask(0), x4[0:BLK], 0.0)
    for j in range(1, 4):
        out = out + jnp.where(_lane_mask(j), x4[j * BLK:(j + 1) * BLK], 0.0)
    return out


def _per_head_rows(vals):
    return jnp.concatenate([jnp.broadcast_to(v, (BLK, 1)) for v in vals], axis=0)


def _lane_mask(j):
    lane = lax.broadcasted_iota(jnp.int32, (1, 256), 1)
    return (lane // HDIM) == j


def _attn_specs(nb):
    blk = lambda off: pl.BlockSpec((BLK, ATW), lambda i: (jnp.clip(i + off, 0, nb - 1) + 2, 0))
    ctx = pl.BlockSpec((L, ATW), lambda i: (0, 0))
    return blk, ctx


def _attn_fwd(qr, k4, v4, sinks, carry=None):
    tt = qr.shape[0]
    s_len = tt - L
    nb = s_len // BLK

    def body(sk_ref, q_ref, kp, ko, kn, kc, vp, vo, vn, vc, y_ref, lse_ref):
        i = pl.program_id(0)
        valid = _attn_masks(i, nb)
        q = q_ref[...]
        ys, lses = [], []
        for g in range(2):
            gs = slice(256 * g, 256 * g + 256)
            kcat = jnp.concatenate([kp[:, gs], ko[:, gs], kn[:, gs], kc[:, gs]], axis=0)
            vcat = jnp.concatenate([vp[:, gs], vo[:, gs], vn[:, gs], vc[:, gs]], axis=0)
            sink4 = _per_head_rows([sk_ref[4 * g + j] for j in range(4)])
            q4 = _stack_heads(q[:, gs])
            o_parts, l_parts = [], []
            for hp in range(2):
                rows = slice(2 * BLK * hp, 2 * BLK * (hp + 1))
                sink = sink4[rows]
                s = jnp.where(valid[rows], _dot(q4[rows], kcat, NT), -1e30)
                m = jnp.maximum(jnp.max(s, axis=-1, keepdims=True), sink)
                e = jnp.exp(s - m)
                den = jnp.sum(e, axis=-1, keepdims=True) + jnp.exp(sink - m)
                o_parts.append(_bdot(e * (1.0 / den), vcat))
                l_parts.append(jnp.broadcast_to(m + jnp.log(den), (2 * BLK, 256)))
            ys.append(_unstack_heads(jnp.concatenate(o_parts, axis=0)))
            lses.append(_unstack_heads(jnp.concatenate(l_parts, axis=0)))
        y_ref[...] = jnp.concatenate(ys, axis=1).astype(BF16)
        lse_ref[...] = jnp.concatenate(lses, axis=1)

    blk, ctx = _attn_specs(nb)
    out = pl.BlockSpec((BLK, ATW), lambda i: (i, 0))
    return _pcall(body, name="attn_fwd", grid=(nb,),
                  in_specs=[pl.BlockSpec(memory_space=pltpu.SMEM), blk(0),
                            blk(-1), blk(0), blk(1), ctx, blk(-1), blk(0), blk(1), ctx],
                  out_specs=[out, out],
                  out_shape=[jax.ShapeDtypeStruct((s_len, ATW), BF16),
                             jax.ShapeDtypeStruct((s_len, ATW), F32)], carry=carry)(
                      sinks, qr, k4, k4, k4, k4, v4, v4, v4, v4)


def _attn_bwd(qr, k4, v4, sinks, y, lse, dy, carry=None):
    tt = qr.shape[0]
    s_len = tt - L
    nb = s_len // BLK

    def body(sk_ref, q_ref, kp, ko, kn, kc, vp, vo, vn, vc, y_ref, lse_ref, dy_ref,
             dq_ref, dkw_ref, dvw_ref, dkc_ref, dvc_ref, dsk_ref):
        i = pl.program_id(0)

        @pl.when(i == 0)
        def _():
            dkc_ref[...] = jnp.zeros_like(dkc_ref)
            dvc_ref[...] = jnp.zeros_like(dvc_ref)
            dsk_ref[...] = jnp.zeros_like(dsk_ref)

        valid = _attn_masks(i, nb)
        q = q_ref[...]
        dy_ = dy_ref[...]
        dly = dy_ * y_ref[...].astype(F32)
        lse_ = lse_ref[...]
        dqs = []
        for g in range(2):
            gs = slice(256 * g, 256 * g + 256)
            kcat = jnp.concatenate([kp[:, gs], ko[:, gs], kn[:, gs], kc[:, gs]], axis=0)
            vcat = jnp.concatenate([vp[:, gs], vo[:, gs], vn[:, gs], vc[:, gs]], axis=0)
            q4 = _stack_heads(q[:, gs])
            dy4 = _stack_heads(dy_[:, gs]).astype(BF16)
            lse4 = jnp.max(_stack_heads(lse_[:, gs], fill=-1e30), axis=-1, keepdims=True)
            delta = jnp.sum(_stack_heads(dly[:, gs]), axis=-1, keepdims=True)
            sink = _per_head_rows([sk_ref[4 * g + j] for j in range(4)])
            pr = jnp.where(valid, jnp.exp(_dot(q4, kcat, NT) - lse4), 0.0)
            dsb = (pr * (_dot(dy4, vcat, NT) - delta)).astype(BF16)
            dsink = jnp.exp(sink - lse4) * delta
            for j in range(4):
                dsk_ref[4 * g + j:4 * g + j + 1, :] += jnp.broadcast_to(
                    -jnp.sum(dsink[j * BLK:(j + 1) * BLK], axis=0, keepdims=True), (1, 128))
            dqs.append(_unstack_heads(_dot(dsb, kcat)))
            dkg = _dot(dsb, q4, TN)
            dvg = _dot(pr.astype(BF16), dy4, TN)
            dkw_ref[0, :, gs] = dkg[:3 * BLK]
            dvw_ref[0, :, gs] = dvg[:3 * BLK]
            dkc_ref[:, gs] += dkg[3 * BLK:]
            dvc_ref[:, gs] += dvg[3 * BLK:]
        dq_ref[...] = jnp.concatenate(dqs, axis=1)

    blk, ctx = _attn_specs(nb)
    out = pl.BlockSpec((BLK, ATW), lambda i: (i, 0))
    win = pl.BlockSpec((1, 3 * BLK, ATW), lambda i: (i, 0, 0))
    acc = _full((L, ATW))
    return _pcall(body, name="attn_bwd", grid=(nb,),
                  in_specs=[pl.BlockSpec(memory_space=pltpu.SMEM), blk(0),
                            blk(-1), blk(0), blk(1), ctx, blk(-1), blk(0), blk(1), ctx, out, out, out],
                  out_specs=[out, win, win, acc, acc, _full((8, 128))],
                  out_shape=[jax.ShapeDtypeStruct((s_len, ATW), F32),
                             jax.ShapeDtypeStruct((nb, 3 * BLK, ATW), F32),
                             jax.ShapeDtypeStruct((nb, 3 * BLK, ATW), F32),
                             jax.ShapeDtypeStruct((L, ATW), F32), jax.ShapeDtypeStruct((L, ATW), F32),
                             jax.ShapeDtypeStruct((8, 128), F32)], carry=carry)(
                      sinks, qr, k4, k4, k4, k4, v4, v4, v4, v4, y, lse, dy)


def _attn_post(p, cos, sin, qnw8, knw2, bd512, bd128, dupt, dq, dkw, dvw, dkc, dvc, dp, carry=None):
    tt = p.shape[0]
    s_len = tt - L
    nb = s_len // BLK
    nctx = L // BLK

    def body(q_ref, kv_ref, cos_ref, sin_ref, qw_ref, kw_ref, b5_ref, b1_ref, dupt_ref,
             dq_ref, kwp, kwo, kwn, vwp, vwo, vwn, dkc_ref, dvc_ref, _dp_in,
             dp_ref, dqw_ref, dkw_ref):
        t = pl.program_id(0)
        j = t - nctx

        @pl.when(t == 0)
        def _():
            dqw_ref[...] = jnp.zeros_like(dqw_ref)
            dkw_ref[...] = jnp.zeros_like(dkw_ref)

        is_lat = t >= nctx
        cos_, sin_ = cos_ref[...], sin_ref[...]
        has_p = is_lat & (j >= 1)
        has_n = is_lat & (j <= nb - 2)
        dk4 = (jnp.where(is_lat, kwo[0], dkc_ref[...]) + jnp.where(has_p, kwp[0], 0.0)
               + jnp.where(has_n, kwn[0], 0.0))
        dv4 = (jnp.where(is_lat, vwo[0], dvc_ref[...]) + jnp.where(has_p, vwp[0], 0.0)
               + jnp.where(has_n, vwn[0], 0.0))
        dkr = _dot(dk4, dupt_ref[...], prec=HI)
        dv = _dot(dv4, dupt_ref[...], prec=HI)
        kv = kv_ref[...]
        k = kv[:, :128]
        kw = kw_ref[...]
        rk = lax.rsqrt(_head_mean(k * k, b1_ref[...]) + EPS)
        xk = k * rk
        dkn = dkr * cos_ + _rot(dkr * sin_)
        dxk = dkn * kw
        dk = rk * (dxk - xk * _head_mean(dxk * xk, b1_ref[...]))
        dkw_ref[...] += jnp.sum(dkn * xk, axis=0, keepdims=True)
        q = q_ref[...]
        qw = qw_ref[...]
        rq = lax.rsqrt(_head_mean(q * q, b5_ref[...]) + EPS)
        xq = q * rq
        cos4 = jnp.concatenate([cos_] * 4, axis=1)
        sin4 = jnp.concatenate([sin_] * 4, axis=1)
        dqr = jnp.where(is_lat, dq_ref[...], 0.0) * (HDIM ** -0.5)
        dqn = dqr * cos4 + _rot(dqr * sin4)
        dxq = dqn * qw
        dqraw = rq * (dxq - xq * _head_mean(dxq * xq, b5_ref[...]))
        dqw_ref[...] += jnp.sum(dqn * xq, axis=0, keepdims=True)
        dp_ref[...] = jnp.concatenate([dqraw, dk, dv], axis=1).astype(BF16)

    row = lambda w, cb: pl.BlockSpec((BLK, w), lambda t: (t, cb))
    lat = pl.BlockSpec((BLK, ATW), lambda t: (jnp.maximum(t - nctx, 0), 0))

    def part(off):
        return pl.BlockSpec((1, BLK, ATW), lambda t: (jnp.clip(t - nctx + off, 0, nb - 1), 1 - off, 0))

    cacc = pl.BlockSpec((BLK, ATW), lambda t: (jnp.minimum(t, nctx - 1), 0))
    return _pcall(body, name="attn_post", grid=(tt // BLK,),
                  in_specs=[row(ATW, C_QRAW), row(256, C_KV), row(128, 0), row(128, 0),
                            _full((1, ATW)), _full((1, 128)), _full((ATW, ATW)), _full((128, 128)),
                            _full((ATW, 128)), lat, part(-1), part(0), part(1), part(-1), part(0), part(1),
                            cacc, cacc, ANY],
                  out_specs=[pl.BlockSpec((BLK, 768), lambda t: (t, C_QKV)), _full((1, ATW)), _full((1, 128))],
                  out_shape=[jax.ShapeDtypeStruct(dp.shape, BF16), jax.ShapeDtypeStruct((1, ATW), F32),
                             jax.ShapeDtypeStruct((1, 128), F32)],
                  aliases={18: 0}, carry=carry)(p, p, cos, sin, qnw8, knw2, bd512, bd128, dupt,
                                   dq, dkw, dkw, dkw, dvw, dvw, dvw, dkc, dvc, dp)


def _branch_merge(y_hg, y_at, bh4, ba4, p):
    s_len = y_hg.shape[0]

    def body(yh_ref, ya_ref, bh_ref, ba_ref, gh_ref, ga_ref, ah_ref, aa_ref, m_ref):
        yh, ya = yh_ref[...], ya_ref[...]
        ah = jnp.concatenate([_bdot(yh, bh_ref[j]) for j in range(4)], axis=1)
        aa = jnp.concatenate([_bdot(ya, ba_ref[j]) for j in range(4)], axis=1)
        ah_ref[...] = ah.astype(BF16)
        aa_ref[...] = aa.astype(BF16)
        m_ref[...] = (_sig(gh_ref[...]) * ah + _sig(ga_ref[...]) * aa).astype(BF16)

    row = pl.BlockSpec((TM, D), lambda i: (i, 0))
    y = pl.BlockSpec((TM, HGW), lambda i: (i, 0))
    f = jax.ShapeDtypeStruct((s_len, D), BF16)
    return _pcall(body, name="branch_merge", grid=(s_len // TM,),
                  in_specs=[y, y, _full(bh4.shape), _full(ba4.shape),
                            pl.BlockSpec((TM, D), lambda i: (i + 1, 2)), pl.BlockSpec((TM, D), lambda i: (i + 1, 3))],
                  out_specs=[row, row, row],
                  out_shape=[f, f, jax.ShapeDtypeStruct((s_len, D), BF16)])(y_hg, y_at, bh4, ba4, p, p)


def _branch_bwd(dmh, dma, bh4, ba4, y_hg, y_at):
    s_len = dmh.shape[0]
    nk = s_len // TS
    ns = D // 4

    def body(dh_ref, da_ref, bh_ref, ba_ref, yh_ref, ya_ref, dyh_ref, dya_ref, gh_ref, ga_ref, acc_h, acc_a):
        t = pl.program_id(0)

        @pl.when(t == 0)
        def _():
            acc_h[...] = jnp.zeros_like(acc_h)
            acc_a[...] = jnp.zeros_like(acc_a)

        for d_ref, w_ref, y_ref, dy_ref, acc in ((dh_ref, bh_ref, yh_ref, dyh_ref, acc_h),
                                                 (da_ref, ba_ref, ya_ref, dya_ref, acc_a)):
            y = y_ref[...]
            dy = jnp.zeros((TS, HGW), F32)
            for j in range(4):
                dj = d_ref[:, j * ns:(j + 1) * ns]
                dy = dy + _bdot(dj, w_ref[j], NT)
                acc[j] += _bdot(y, dj, TN)
            dy_ref[...] = dy

        @pl.when(t == nk - 1)
        def _():
            gh_ref[...] = acc_h[...].astype(BF16)
            ga_ref[...] = acc_a[...].astype(BF16)

    dm = pl.BlockSpec((TS, D), lambda t: (t, 0))
    y = pl.BlockSpec((TS, HGW), lambda t: (t, 0))
    w = _full(bh4.shape)
    fy = jax.ShapeDtypeStruct((s_len, HGW), F32)
    gw = jax.ShapeDtypeStruct(bh4.shape, BF16)
    return _pcall(body, name="branch_bwd", grid=(nk,), in_specs=[dm, dm, w, w, y, y],
                  out_specs=[y, y, w, w], out_shape=[fy, fy, gw, gw],
                  scratch=[pltpu.VMEM(bh4.shape, F32)] * 2)(dmh, dma, bh4, ba4, y_hg, y_at)


def _merge_bwd(dattn, w_o, mixed, ah, aa, p, carry=None):
    tt = p.shape[0]
    s_len = tt - L
    nt = tt // TM

    def body(da_ref, wo_ref, mx_ref, ah_ref, aa_ref, gh_ref, ga_ref, dp_ref, dmh_ref, dma_ref, go_ref, acc):
        i = pl.program_id(0)

        @pl.when(i == 0)
        def _():
            dp_ref[...] = jnp.zeros_like(dp_ref)
            acc[...] = jnp.zeros_like(acc)

        @pl.when(i >= 1)
        def _():
            da = da_ref[...]
            acc[...] += _bdot(mx_ref[...], da, TN)
            dm_ = _bdot(da, wo_ref[...], NT)
            sh, sa = _sig(gh_ref[...]), _sig(ga_ref[...])
            dp_ref[...] = jnp.concatenate([dm_ * ah_ref[...].astype(F32) * sh * (1.0 - sh),
                                           dm_ * aa_ref[...].astype(F32) * sa * (1.0 - sa)], axis=1).astype(BF16)
            dmh_ref[...] = (dm_ * sh).astype(BF16)
            dma_ref[...] = (dm_ * sa).astype(BF16)

        @pl.when(i == nt - 1)
        def _():
            go_ref[...] = acc[...].astype(BF16)

    lat = pl.BlockSpec((TM, D), lambda i: (jnp.maximum(i - 1, 0), 0))
    return _pcall(body, name="merge_bwd", grid=(nt,),
                  in_specs=[lat, _full((D, D)), lat, lat, lat, pl.BlockSpec((TM, D), lambda i: (i, 2)),
                            pl.BlockSpec((TM, D), lambda i: (i, 3))],
                  out_specs=[pl.BlockSpec((TM, 2 * D), lambda i: (i, C_GATES)), lat, lat, _full((D, D))],
                  out_shape=[jax.ShapeDtypeStruct((tt, NCOL), BF16), jax.ShapeDtypeStruct((s_len, D), BF16),
                             jax.ShapeDtypeStruct((s_len, D), BF16), jax.ShapeDtypeStruct((D, D), BF16)],
                  scratch=[pltpu.VMEM((D, D), F32)], carry=carry)(dattn, w_o, mixed, ah, aa, p, p)


def _local_step(x, ctx, tgt, mod, modc, nw1, nw2, lg, hw, qnw, knw, sinks,
                w_in, wts, dist=None):
    s_len = x.shape[0]
    tt = s_len + L
    ss1 = jnp.stack([modc, mod[0:2]])
    ss2 = mod[3:5][None]
    g1, g2 = mod[2:3], mod[5:6]
    hw4 = jnp.tile(hw, (1, 4))
    qnw8 = jnp.tile(qnw, (1, 8))
    knw2 = jnp.tile(knw, (1, 2))
    cos, sin = _rope_tables(s_len)
    bd512, bd128 = _blockdiag(ATW, HDIM), _blockdiag(128, HDIM)
    dupm = _dup_matrix()
    dup, dupt = jnp.asarray(dupm, BF16), jnp.asarray(dupm.T, F32)
    tmt = tt

    def four(b):
        return b.reshape(4, 2 * b.shape[1], b.shape[2])

    def halves(g):
        return g.reshape(4, 2, g.shape[1] // 2, g.shape[2])

    h = _mod1(ctx, x, nw1, ss1)
    if dist is None:
        bh4, ba4, w_o, g4, u4, dn4 = wts
        p = _mm_in(h, w_in, tmt)
        o0, st0 = _hgrn_fwd(p, lg, rev=False)
        o1, st1, y_hg = _hgrn_fwd(p, lg, rev=True, readout=(o0, hw4))
    else:
        core, chip = dist
        half = wts[3].shape[1] // 2
        p, first = _mm_in(h, w_in, tmt, carry=_carry_join(_carry_gather(list(wts[0:3])),
                                                          _carry_gather([wts[3]], rows=[(0, half)])))
        (o0, st0), (g8,) = _hgrn_fwd(p, lg, rev=False, carry=_carry_gather([first[3]], rows=[(half, half)]))
        (o1, st1, y_hg), (dn8a,) = _hgrn_fwd(p, lg, rev=True, readout=(o0, hw4),
                                             carry=_carry_gather([wts[5]], rows=[(0, half)]))
        bh4, ba4, w_o, g4 = four(first[0]), four(first[1]), four(first[2]).reshape(D, D), four(g8)
    qr, k4, v4 = _qk_prep(p, cos, sin, qnw8, knw2, bd512, bd128, dup)
    if dist is None:
        y_at, lse = _attn_fwd(qr, k4, v4, sinks)
    else:
        (y_at, lse), (u8, dn8) = _attn_fwd(qr, k4, v4, sinks,
                                           carry=_carry_gather([wts[4], dn8a], rows=[None, (half, half)]))
        u4, dn4 = four(u8), four(dn8)
    ah, aa, mixed = _branch_merge(y_hg, y_at, bh4, ba4, p)
    ao, x1, h2 = _out_proj_mod2(mixed, w_o, x, g1, nw2, ss2)
    a4, b4, z4 = _ffn_up(h2, g4, u4)
    sq, dx2, dyb, dg2 = _ffn_down_loss(z4, dn4, x1, g2, tgt)

    da4, db4 = _ffn_dz(dyb, dn4, a4, b4)
    g_dn = _ffn_gdn(z4, dyb)
    if dist is None:
        dh2 = _ffn_dh2(da4, db4, g4, u4)
    else:
        dn_units = [halves(g_dn)]
        dh2, dn_recv = _ffn_dh2(da4, db4, g4, u4, carry=_carry_pairx(dn_units))
        dn_pairs = _rs_pair_add(dn_units, dn_recv, core)
    if dist is None:
        g_g, g_u = _ffn_ggu(h2, da4, db4)
    else:
        (g_g, g_u), c_dn = _ffn_ggu(h2, da4, db4, carry=_carry_chipx(dn_pairs))
        red_dn = _rs_chip_add(dn_pairs, c_dn, core, chip)
    dx1, dattn, dss2, dnw2, dg1 = _mod2_bwd(x1, dh2, dx2, ao, nw2, ss2, g1)
    if dist is None:
        dp, dmh, dma, g_o = _merge_bwd(dattn, w_o, mixed, ah, aa, p)
    else:
        gu_units = [halves(g_g), halves(g_u)]
        (dp, dmh, dma, g_o), gu_recv = _merge_bwd(dattn, w_o, mixed, ah, aa, p, carry=_carry_pairx(gu_units))
        ffn_pairs = list(dn_pairs) + list(_rs_pair_add(gu_units, gu_recv, core))
    dy_hg, dy_at, g_bh, g_ba = _branch_bwd(dmh, dma, bh4, ba4, y_hg, y_at)
    if dist is None:
        dp, do, dhw4 = _readout_bwd(o0, o1, p, hw4, dy_hg, dp)
        dq, dkw, dvw, dkc, dvc, dsk = _attn_bwd(qr, k4, v4, sinks, y_at, lse, dy_at)
        dp, dqnw8, dknw2 = _attn_post(p, cos, sin, qnw8, knw2, bd512, bd128, dupt, dq, dkw, dvw, dkc, dvc, dp)
    else:
        mix_units = [halves(g_bh), halves(g_ba), halves(g_o.reshape(4, D // 4, D))]
        (dp, do, dhw4), mix_recv = _readout_bwd(o0, o1, p, hw4, dy_hg, dp, carry=_carry_pairx(mix_units))
        mix_pairs = _rs_pair_add(mix_units, mix_recv, core)
        (dq, dkw, dvw, dkc, dvc, dsk), bwd = _attn_bwd(
            qr, k4, v4, sinks, y_at, lse, dy_at,
            carry=_carry_join(_carry_chipx(ffn_pairs[1:2]), _carry_sibx(red_dn)))
        red_g = _rs_chip_add(ffn_pairs[1:2], bwd[0:1], core, chip)
        (dp, dqnw8, dknw2), post = _attn_post(
            p, cos, sin, qnw8, knw2, bd512, bd128, dupt, dq, dkw, dvw, dkc, dvc, dp, carry=_carry_sibx(red_g))
    if dist is None:
        dp, dv0, dq0, dlg0 = _hgrn_bwd(p, lg, do, st0, dp, None, rev=False)
        dp, dlg1 = _hgrn_bwd(p, lg, do, st1, dp, (dv0, dq0), rev=True)
    else:
        (dp, dv0, dq0, dlg0), c_u = _hgrn_bwd(p, lg, do, st0, dp, None, rev=False,
                                              carry=_carry_chipx(ffn_pairs[2:3]))
        red_u = _rs_chip_add(ffn_pairs[2:3], c_u, core, chip)
        (dp, dlg1), last = _hgrn_bwd(p, lg, do, st1, dp, (dv0, dq0), rev=True,
                                     carry=_carry_join(_carry_chipx(mix_pairs), _carry_sibx(red_u)))
        mix_reds = _rs_chip_add(mix_pairs, last[0:3], core, chip)
        ffn_done = bwd[1:2] + post[0:1] + last[3:4]
    g_in = _mm_gin(dp, h, tmt)
    if dist is None:
        dh = _mm_dh(dp, w_in, tmt)
        gx, dss1, dnw1 = _mod1_bwd(ctx, x, dh, dx1, nw1, ss1)
        rs = None
    else:
        in_units = [halves(g_in.reshape(4, NCOL // 4, D))]
        in_pairs = _rs_pair_add(in_units, _pair_exchange(in_units), core)
        first_rows, rest_rows = _split_rows(in_pairs[0].shape[1], IN_ROWS_WITH_DH)
        dh, both = _mm_dh(dp, w_in, tmt, carry=_carry_join(_carry_chipx(in_pairs, rows=first_rows),
                                                           _carry_sibx(mix_reds)))
        in_part, mix_done = both[0:1], both[1:4]
        gx, dss1, dnw1 = _mod1_bwd(ctx, x, dh, dx1, nw1, ss1)
        rs = dict(ffn_done=ffn_done, mix_done=mix_done, in_pairs=in_pairs, in_part=in_part, rest_rows=rest_rows)

    dmod = jnp.concatenate([dss1[1], dg1, dss2, dg2], axis=0)
    dmodc = dss1[0]
    raw = (dss1, dg1, dss2, dg2, dnw1, dnw2, dhw4, dqnw8, dknw2, dsk, dlg0, dlg1)
    small = dict(raw=raw, dmod=dmod, dmodc=dmodc, dnw1=dnw1, dnw2=dnw2,
                 dhw=dhw4.reshape(4, HGD).sum(0, keepdims=True),
                 dqnw=dqnw8.reshape(8, HDIM).sum(0, keepdims=True),
                 dknw=dknw2.reshape(2, HDIM).sum(0, keepdims=True),
                 dsinks=dsk[:, 0], dlg=jnp.concatenate([dlg0, dlg1], axis=0))
    big = dict(w_in=g_in, w_bh=g_bh, w_ba=g_ba, w_o=g_o, w_g=g_g, w_u=g_u, w_dn=g_dn)
    return sq, gx, big, small, rs


def _place():
    x, y, c = lax.axis_index("x"), lax.axis_index("y"), lax.axis_index("c")
    return x, y, c


def _gather_blocks(x_refs, out_refs, send_sems, recv_sems, local_sems):
    n = len(out_refs)
    x, y, c = _place()
    me, sibling = (x, y, c), (x, y, 1 - c)
    chips = [(1 - x, y), (x, 1 - y), (1 - x, 1 - y)]

    def slot(u, px, py, pc):
        return out_refs[u].at[4 * px + 2 * py + pc]

    def copy(u, k, block, to, src=None):
        return pltpu.make_async_remote_copy(
            src_ref=slot(u, *block) if src is None else src, dst_ref=slot(u, *block),
            send_sem=send_sems.at[u, k], recv_sem=recv_sems.at[u, k], device_id=to, device_id_type=MESH)

    mines = [pltpu.make_async_copy(x_refs[u], slot(u, *me), local_sems.at[u]) for u in range(n)]
    for cp in mines:
        cp.start()
    first = []
    for u in range(n):
        first.append(copy(u, 0, me, sibling, src=x_refs[u]))
        first += [copy(u, 1 + j, me, (*chip, c), src=x_refs[u]) for j, chip in enumerate(chips)]
    for cp in first:
        cp.start()
    passed = []
    for j, chip in enumerate(chips):
        for u in range(n):
            copy(u, 1 + j, (*chip, c), me).wait_recv()
            fwd = copy(u, 4 + j, (*chip, c), sibling)
            fwd.start()
            passed.append(fwd)
    for u in range(n):
        copy(u, 0, sibling, me).wait_recv()
    for j, chip in enumerate(chips):
        for u in range(n):
            copy(u, 4 + j, (*chip, 1 - c), me).wait_recv()
    for cp in first + passed:
        cp.wait_send()
    for cp in mines:
        cp.wait()


def _gather_sems(n):
    return [pltpu.SemaphoreType.DMA((n, 7)), pltpu.SemaphoreType.DMA((n, 7)), pltpu.SemaphoreType.DMA((n,))]


def _allgather(blks, *, name):
    n = len(blks)
    vm = pl.BlockSpec(memory_space=pltpu.VMEM)

    def body(*refs):
        _peer_barrier("both")
        _gather_blocks(refs[:n], refs[n:2 * n], *refs[2 * n:])

    return pl.pallas_call(
        body, name=name, out_shape=[jax.ShapeDtypeStruct((8,) + b.shape, b.dtype) for b in blks],
        in_specs=[vm] * n, out_specs=[vm] * n, scratch_shapes=_gather_sems(n),
        compiler_params=pltpu.CompilerParams(collective_id=BARRIER_IDS["both"]))(*blks)


def _cast_place(ws, c, dev):
    n = len(ws)

    def body(s_ref, *refs):
        for u in range(n):
            refs[n + u][0] = refs[u][...].astype(BF16)

    in_specs, out_specs, out_shape = [], [], []
    for w in ws:
        q, cols = w.shape[0] // 4, w.shape[1]
        in_specs.append(pl.BlockSpec((q, cols), lambda i, s: (2 * s[0] + i, 0)))
        out_specs.append(pl.BlockSpec((1, q, cols), lambda i, s: (s[1], i, 0)))
        out_shape.append(jax.ShapeDtypeStruct((8, 2 * q, cols), BF16))
    return pl.pallas_call(
        body, name="cast_place",
        grid_spec=pltpu.PrefetchScalarGridSpec(num_scalar_prefetch=1, grid=(2,), in_specs=in_specs,
                                               out_specs=out_specs),
        out_shape=_out_hbm(out_shape),
        compiler_params=pltpu.CompilerParams(vmem_limit_bytes=48 << 20))(jnp.stack([c, dev]), *_in_hbm(ws))


def _gather_phases(out_refs, send_sems, recv_sems, rows=None):
    n = len(out_refs)
    x, y, c = _place()
    me, sibling = (x, y, c), (x, y, 1 - c)
    chips = [(1 - x, y), (x, 1 - y), (1 - x, 1 - y)]

    def copy(u, k, block, to):
        px, py, pc = block
        ref = out_refs[u].at[4 * px + 2 * py + pc]
        if rows is not None and rows[u] is not None:
            ref = ref.at[pl.ds(rows[u][0], rows[u][1])]
        return pltpu.make_async_remote_copy(src_ref=ref, dst_ref=ref, send_sem=send_sems.at[u, k],
                                            recv_sem=recv_sems.at[u, k], device_id=to, device_id_type=MESH)

    def start():
        for u in range(n):
            copy(u, 0, me, sibling).start()
            for j, chip in enumerate(chips):
                copy(u, 1 + j, me, (*chip, c)).start()

    def mid():
        for j, chip in enumerate(chips):
            for u in range(n):
                copy(u, 1 + j, (*chip, c), me).wait_recv()
                copy(u, 4 + j, (*chip, c), sibling).start()

    def end():
        for u in range(n):
            copy(u, 0, sibling, me).wait_recv()
        for j, chip in enumerate(chips):
            for u in range(n):
                copy(u, 4 + j, (*chip, 1 - c), me).wait_recv()
        for u in range(n):
            copy(u, 0, me, sibling).wait_send()
            for j, chip in enumerate(chips):
                copy(u, 1 + j, me, (*chip, c)).wait_send()
                copy(u, 4 + j, (*chip, c), sibling).wait_send()

    return start, mid, end


def _carry_gather(bufs, rows=None):
    n = len(bufs)
    return _Carry(bufs, [jax.ShapeDtypeStruct(b.shape, b.dtype) for b in bufs], {u: u for u in range(n)},
                  [pltpu.SemaphoreType.DMA((n, 7)), pltpu.SemaphoreType.DMA((n, 7))],
                  lambda ins, outs, sems: _gather_phases(outs, *sems, rows=rows), "both")


def _ag_small(raw, sq):
    def body(dss1, dg1, dss2, dg2, dnw1, dnw2, dhw4, dqnw8, dknw2, dsk, dlg0, dlg1, sq_ref,
             out_ref, tot_ref, blk, send_sems, recv_sems, local_sems):
        _peer_barrier("both")
        blk[...] = jnp.zeros_like(blk)
        blk[0:2, :] = dss1[1]
        blk[2:3, :] = dg1[...]
        blk[3:5, :] = dss2[...]
        blk[5:6, :] = dg2[...]
        blk[6:8, :] = dss1[0]
        blk[8:9, :] = dnw1[...]
        blk[9:10, :] = dnw2[...]
        blk[10:11, 0:HGW] = dhw4[...]
        blk[10:11, HGW:D] = dqnw8[...]
        blk[11:12, 0:128] = dknw2[...]
        blk[12:14, 0:HGW] = dlg0[0]
        blk[14:16, 0:HGW] = dlg1[0]
        blk[16:24, 0:128] = dsk[...]
        blk[24:25, :] = sq_ref[...]
        _gather_blocks([blk], [out_ref], send_sems, recv_sems, local_sems)
        acc = out_ref[0]
        for i in range(1, 8):
            acc = acc + out_ref[i]
        tot_ref[...] = acc

    vm = pl.BlockSpec(memory_space=pltpu.VMEM)
    return pl.pallas_call(
        body, name="ag_small",
        out_shape=[jax.ShapeDtypeStruct((8, 32, D), F32), jax.ShapeDtypeStruct((32, D), F32)],
        in_specs=[vm] * 13, out_specs=[vm, vm],
        scratch_shapes=[pltpu.VMEM((32, D), F32)] + _gather_sems(1),
        compiler_params=pltpu.CompilerParams(collective_id=BARRIER_IDS["both"]))(*raw, sq)


def _pairx_shapes(units):
    return [jax.ShapeDtypeStruct((4,) + g.shape[2:], g.dtype) for g in units]


def _pairx_phases(g_refs, r_refs, send_sems, recv_sems):
    n = len(g_refs)
    x, y, c = _place()
    cps = [pltpu.make_async_remote_copy(
        src_ref=g_refs[u].at[j, 1 - c], dst_ref=r_refs[u].at[j], send_sem=send_sems.at[u, j],
        recv_sem=recv_sems.at[u, j], device_id=(x, y, 1 - c), device_id_type=MESH)
        for u in range(n) for j in range(4)]

    def start():
        for cp in cps:
            cp.start()

    def end():
        for cp in cps:
            cp.wait()

    return start, None, end


def _carry_pairx(units):
    n = len(units)
    return _Carry(units, _pairx_shapes(units), {},
                  [pltpu.SemaphoreType.DMA((n, 4)), pltpu.SemaphoreType.DMA((n, 4))],
                  lambda ins, outs, sems: _pairx_phases(ins, outs, *sems), "sib")


def _pair_exchange(units):
    n = len(units)

    def body(*refs):
        _peer_barrier("sib")
        start, _, end = _pairx_phases(refs[:n], refs[n:2 * n], *refs[2 * n:])
        start()
        end()

    return pl.pallas_call(
        body, name="pair_exchange", out_shape=_pairx_shapes(units), in_specs=[ANY] * n, out_specs=[ANY] * n,
        scratch_shapes=[pltpu.SemaphoreType.DMA((n, 4))] * 2,
        compiler_params=pltpu.CompilerParams(collective_id=BARRIER_IDS["sib"]))(*units)


IN_ROWS_WITH_DH = 0.6


def _split_rows(h, share):
    first = int(h * share) // BF16_SUBLANES * BF16_SUBLANES
    return (0, first), (first, h - first)


def _rs_pair_add(units, recvs, c):
    n = len(units)

    def body(c_ref, *refs):
        for u in range(n):
            refs[2 * n + u][...] = (refs[u][0].astype(F32) + refs[n + u][...].astype(F32)).astype(BF16)

    in_specs, out_specs, out_shape = [], [], []
    for g in units:
        h, w = g.shape[2:]
        in_specs.append(pl.BlockSpec((1, 1, h, w), lambda j, cr: (j, cr[0], 0, 0)))
    for g in units:
        h, w = g.shape[2:]
        in_specs.append(pl.BlockSpec((1, h, w), lambda j, cr: (j, 0, 0)))
        out_specs.append(pl.BlockSpec((1, h, w), lambda j, cr: (j, 0, 0)))
        out_shape.append(jax.ShapeDtypeStruct((4, h, w), BF16))
    return pl.pallas_call(
        body, name="rs_pair_add",
        grid_spec=pltpu.PrefetchScalarGridSpec(num_scalar_prefetch=1, grid=(4,), in_specs=in_specs,
                                               out_specs=out_specs),
        out_shape=_out_hbm(out_shape),
        compiler_params=pltpu.CompilerParams(vmem_limit_bytes=48 << 20))(
            c.reshape(1), *_in_hbm(list(units) + list(recvs)))


def _chipx_phases(p_refs, r_refs, send_sems, recv_sems, rows=None):
    n = len(p_refs)
    x, y, c = _place()
    k = 2 * x + y

    def part(ref):
        return ref if rows is None else ref.at[pl.ds(rows[0], rows[1])]

    sends = []
    for d in range(1, 4):
        j = (k + d) % 4
        for u in range(n):
            sends.append(pltpu.make_async_remote_copy(
                src_ref=part(p_refs[u].at[j]), dst_ref=part(r_refs[u].at[k]), send_sem=send_sems.at[u, d - 1],
                recv_sem=recv_sems.at[u, d - 1], device_id=(j // 2, j % 2, c), device_id_type=MESH))

    def start():
        for cp in sends:
            cp.start()

    def end():
        for d in range(1, 4):
            src = (k + 4 - d) % 4
            for u in range(n):
                pltpu.make_async_remote_copy(
                    src_ref=part(p_refs[u].at[src]), dst_ref=part(r_refs[u].at[src]),
                    send_sem=send_sems.at[u, d - 1], recv_sem=recv_sems.at[u, d - 1], device_id=(x, y, c),
                    device_id_type=MESH).wait_recv()
        for cp in sends:
            cp.wait_send()

    return start, None, end


def _carry_chipx(pairs, rows=None, into=None):
    n = len(pairs)
    sems = [pltpu.SemaphoreType.DMA((n, 3)), pltpu.SemaphoreType.DMA((n, 3))]
    shapes = [jax.ShapeDtypeStruct(p.shape, p.dtype) for p in pairs]
    if into is None:
        return _Carry(pairs, shapes, {}, sems, lambda ins, outs, s: _chipx_phases(ins, outs, *s, rows=rows),
                      "chips")
    return _Carry(list(pairs) + list(into), shapes, {n + u: u for u in range(n)}, sems,
                  lambda ins, outs, s: _chipx_phases(ins[:n], outs, *s, rows=rows), "chips")


def _rs_chip_add(pairs, contribs, c, chip):
    n = len(pairs)

    def body(s_ref, *refs):
        for u in range(n):
            a, b, c_, d = refs[4 * u:4 * u + 4]
            refs[4 * n + u][0] = ((a[0].astype(F32) + b[0].astype(F32)) + c_[0].astype(F32)) + d[0].astype(F32)

    in_specs, out_specs, out_shape, args = [], [], [], []
    for p, r in zip(pairs, contribs):
        h, w = p.shape[1] // 2, p.shape[2]
        in_specs += [pl.BlockSpec((1, h, w), functools.partial(lambda d, i, s: ((s[1] + d) % 4, i, 0), d))
                     for d in range(4)]
        args += [p, r, r, r]
        out_specs.append(pl.BlockSpec((1, h, w), lambda i, s: (s[0], i, 0)))
        out_shape.append(jax.ShapeDtypeStruct((2, 2 * h, w), F32))
    return pl.pallas_call(
        body, name="rs_chip_add",
        grid_spec=pltpu.PrefetchScalarGridSpec(num_scalar_prefetch=1, grid=(2,), in_specs=in_specs,
                                               out_specs=out_specs),
        out_shape=_out_hbm(out_shape),
        compiler_params=pltpu.CompilerParams(vmem_limit_bytes=48 << 20))(jnp.stack([c, chip]), *_in_hbm(args))


def _rs_sibling_gather(reds):
    n = len(reds)

    def body(*refs):
        _peer_barrier("sib")
        start, _, end = _sibx_phases(refs[n:2 * n], *refs[2 * n:])
        start()
        end()

    return pl.pallas_call(
        body, name="rs_sibling_gather", out_shape=[jax.ShapeDtypeStruct(r.shape, r.dtype) for r in reds],
        in_specs=[ANY] * n, out_specs=[ANY] * n, input_output_aliases={u: u for u in range(n)},
        scratch_shapes=[pltpu.SemaphoreType.DMA((n,))] * 2,
        compiler_params=pltpu.CompilerParams(collective_id=BARRIER_IDS["sib"]))(*reds)


def _sibx_phases(o_refs, send_sems, recv_sems):
    n = len(o_refs)
    x, y, c = _place()
    cps = [pltpu.make_async_remote_copy(
        src_ref=o_refs[u].at[c], dst_ref=o_refs[u].at[c], send_sem=send_sems.at[u], recv_sem=recv_sems.at[u],
        device_id=(x, y, 1 - c), device_id_type=MESH) for u in range(n)]

    def start():
        for cp in cps:
            cp.start()

    def end():
        for u in range(n):
            cps[u].wait_send()
            pltpu.make_async_remote_copy(
                src_ref=o_refs[u].at[1 - c], dst_ref=o_refs[u].at[1 - c], send_sem=send_sems.at[u],
                recv_sem=recv_sems.at[u], device_id=(x, y, 1 - c), device_id_type=MESH).wait_recv()

    return start, None, end


def _carry_sibx(reds):
    n = len(reds)
    return _Carry(reds, [jax.ShapeDtypeStruct(r.shape, r.dtype) for r in reds], {u: u for u in range(n)},
                  [pltpu.SemaphoreType.DMA((n,))] * 2, lambda ins, outs, sems: _sibx_phases(outs, *sems), "sib")


def _prologue(blk, c_ctx, w, b, in8):
    n = w.shape[1]

    def body(blk_ref, cctx_ref, w_ref, b_ref, _in_in, g0_ref, c16_ref, g1_ref, in_ref, s1, r1, l1, s2, r2, s3, r3):
        _peer_barrier("both")
        x, y, c = _place()
        start, mid, end = _gather_phases([in_ref], s3, r3)
        start_mod, mid_mod, end_mod = _gather_phases([g1_ref], s2, r2)
        _gather_blocks([blk_ref], [g0_ref], s1, r1, l1)
        start()
        c16 = jnp.concatenate([g0_ref[i, 0:1, :] for i in range(8)] + [cctx_ref[...], jnp.zeros((7, D), F32)],
                              axis=0)
        c16_ref[...] = c16
        g1_ref[4 * x + 2 * y + c] = _dot(c16 * _sig(c16), w_ref[...], prec=HI) + b_ref[...]
        start_mod()
        mid()
        mid_mod()
        end()
        end_mod()

    vm = pl.BlockSpec(memory_space=pltpu.VMEM)
    return pl.pallas_call(
        body, name="prologue",
        out_shape=[jax.ShapeDtypeStruct((8, 8, D), F32), jax.ShapeDtypeStruct((16, D), F32),
                   jax.ShapeDtypeStruct((8, 16, n), F32), jax.ShapeDtypeStruct(in8.shape, in8.dtype)],
        in_specs=[vm, vm, vm, vm, ANY], out_specs=[vm, vm, vm, ANY], input_output_aliases={4: 3},
        scratch_shapes=_gather_sems(1) + [pltpu.SemaphoreType.DMA((1, 7))] * 4,
        compiler_params=pltpu.CompilerParams(vmem_limit_bytes=48 << 20,
                                             collective_id=BARRIER_IDS["both"]))(blk, c_ctx, w, b, in8)


def _ada_bwd(c16, dmod16, w, carry=None):
    n = w.shape[1]
    tn = 512

    def body(c_ref, d_ref, w_ref, gw_ref, gc_ref):
        j = pl.program_id(0)

        @pl.when(j == 0)
        def _():
            gc_ref[...] = jnp.zeros_like(gc_ref)

        cc = c_ref[...]
        dm = d_ref[...]
        gw_ref[...] = _dot(cc * _sig(cc), dm, TN, prec=HI)
        gc_ref[...] += _dot(dm, w_ref[...], NT, prec=HI)

    return _pcall(body, name="ada_bwd", grid=(n // tn,),
                  in_specs=[_full((16, D)), pl.BlockSpec((16, tn), lambda j: (0, j)),
                            pl.BlockSpec((D, tn), lambda j: (0, j))],
                  out_specs=[pl.BlockSpec((D, tn), lambda j: (0, j)), _full((16, D))],
                  out_shape=[jax.ShapeDtypeStruct((D, n), F32),
                             jax.ShapeDtypeStruct((16, D), F32)], carry=carry)(c16, dmod16, w)


def _adam_math(w, g, m, v):
    c1 = 1.0 - ADAM_B1 ** ADAM_STEP
    c2 = 1.0 - ADAM_B2 ** ADAM_STEP
    nm = ADAM_B1 * m + (1.0 - ADAM_B1) * g
    nv = ADAM_B2 * v + (1.0 - ADAM_B2) * (g * g)
    return -ADAM_LR * ((nm / c1) / (jnp.sqrt(nv / c2) + ADAM_EPS) + ADAM_WD * w), nm, nv


def _adamw_small(ws, gs, ms, vs):
    n = len(ws)

    def body(*refs):
        for u in range(n):
            d_, nm, nv = _adam_math(refs[u][...], refs[n + u][...], refs[2 * n + u][...], refs[3 * n + u][...])
            refs[4 * n + u][...] = d_
            refs[5 * n + u][...] = nm
            refs[6 * n + u][...] = nv

    specs = [_full(w.shape) for w in ws]
    shapes = [jax.ShapeDtypeStruct(w.shape, F32) for w in ws]
    out = _pcall(body, name="adamw_small", grid=(1,), in_specs=specs * 4, out_specs=specs * 3,
                 out_shape=shapes * 3)(*ws, *gs, *ms, *vs)
    return out[:n], out[n:2 * n], out[2 * n:]


def _cctx_grad(parts, c_ctx):
    def body(p_ref, c_ref, o_ref):
        acc = p_ref[0:1, :]
        for k in range(1, 4):
            acc = acc + p_ref[k:k + 1, :]
        cc = c_ref[...]
        s = _sig(cc)
        o_ref[...] = acc * (s * (1.0 + cc * (1.0 - s)))

    return _pcall(body, name="cctx_grad", grid=(1,), in_specs=[_full(parts.shape), _full((1, D))],
                  out_specs=_full((1, D)), out_shape=jax.ShapeDtypeStruct((1, D), F32))(parts, c_ctx)


ADAM_STEPS = 8


def _adamw_multi(ws, gs, ms, vs, *, name, carry=None):
    n = len(ws)

    def body(*refs):
        for u in range(n):
            refs[4 * n + u][...], refs[5 * n + u][...], refs[6 * n + u][...] = _adam_math(
                refs[u][...], refs[n + u][...], refs[2 * n + u][...], refs[3 * n + u][...])

    specs = [pl.BlockSpec((w.shape[0] // ADAM_STEPS, w.shape[1]), lambda i: (i, 0)) for w in ws]
    shapes = [jax.ShapeDtypeStruct(w.shape, F32) for w in ws]
    res = _pcall(body, name=name, grid=(ADAM_STEPS,), in_specs=specs * 4, out_specs=specs * 3,
                 out_shape=shapes * 3, carry=carry)(*ws, *gs, *ms, *vs)
    out, extra = res if carry is not None else (res, None)
    return (out[:n], out[n:2 * n], out[2 * n:]), extra


def kernel(x, c, ctx, c_ctx, w_ada, b_ada, norm_mix_w, norm_ffn_w, w_in, hgrn_lb_logits, hgrn_norm_w, q_norm_w, k_norm_w, attn_sinks, w_branch_hgrn, w_branch_attn, w_out, w_ffn_gate, w_ffn_up, w_ffn_down, loss_target, m_c_ctx, m_w_ada, m_b_ada, m_norm_mix_w, m_norm_ffn_w, m_w_in, m_hgrn_lb_logits, m_hgrn_norm_w, m_q_norm_w, m_k_norm_w, m_attn_sinks, m_w_branch_hgrn, m_w_branch_attn, m_w_out, m_w_ffn_gate, m_w_ffn_up, m_w_ffn_down, v_c_ctx, v_w_ada, v_b_ada, v_norm_mix_w, v_norm_ffn_w, v_w_in, v_hgrn_lb_logits, v_hgrn_norm_w, v_q_norm_w, v_k_norm_w, v_attn_sinks, v_w_branch_hgrn, v_w_branch_attn, v_w_out, v_w_ffn_gate, v_w_ffn_up, v_w_ffn_down):
    xi, yi, ci = _place()
    chip = 2 * xi + yi
    dev = 2 * chip + ci
    s_len = x.shape[1]

    shards = [w_in[0].T, w_branch_hgrn[0], w_branch_attn[0], w_out[0], w_ffn_gate[0].T, w_ffn_up[0].T,
              w_ffn_down[0]]
    bufs = _cast_place(shards, ci, dev)

    lbrow = jnp.pad(hgrn_lb_logits.reshape(1, 512), ((0, 0), (0, D - 512)))
    blk = jnp.concatenate([c, lbrow, jnp.zeros((6, D), F32)], axis=0)
    nada = w_ada.shape[2]
    b_sh = lax.dynamic_slice(b_ada, (0, chip * nada), (1, nada))
    g0, c16, g1, in8 = _prologue(blk, c_ctx[None], w_ada[0], b_sh, bufs[0])
    lg = g0[0::2, 1, :512].reshape(4, 2, 2, 128).transpose(1, 2, 0, 3).reshape(2, 2, HGW)
    modall = g1[0::2].transpose(1, 0, 2).reshape(16, 4 * nada)
    mod = lax.dynamic_slice(modall, (dev, 0), (1, 6 * D)).reshape(6, D)
    modc = modall[8].reshape(6, D)[:2]

    sq, gx, _, small, rs = _local_step(
        x[0], ctx[0], loss_target[0], mod, modc, norm_mix_w, norm_ffn_w, lg, hgrn_norm_w, q_norm_w,
        k_norm_w, attn_sinks[0], in8.reshape(NCOL, D), bufs[1:], dist=(ci, chip))

    def whole(r):
        return r.reshape(2 * r.shape[1], r.shape[2])

    g_dn, g_g, g_u = [whole(r) for r in rs["ffn_done"]]
    g_bh, g_ba, g_o = [whole(r) for r in rs["mix_done"]]
    in_pairs = rs["in_pairs"]

    g2, tot = _ag_small(small["raw"], sq)
    loss = 0.5 * jnp.sum(tot[24]) / D
    dmodc_tot = jnp.pad(tot[6:8].reshape(1, 2 * D), ((0, 0), (0, 4 * D)))
    g_b_ada = tot[0:6].reshape(1, 6 * D) + dmodc_tot
    dmod16 = jnp.concatenate([g2[:, 0:6].reshape(8, 6 * D), dmodc_tot, jnp.zeros((7, 6 * D), F32)], axis=0)
    (g_w_ada, gc_part), in_contribs = _ada_bwd(
        c16, lax.dynamic_slice(dmod16, (0, chip * nada), (16, nada)), w_ada[0],
        carry=_carry_chipx(in_pairs, rows=rs["rest_rows"], into=rs["in_part"]))
    g3, = _allgather([gc_part[8:16]], name="ag_cctx")
    g_c_ctx = _cctx_grad(g3[0::2, 0], c_ctx[None])[0]
    g_nw1 = tot[8:9]
    g_nw2 = tot[9:10]
    g_hw = tot[10, :HGW].reshape(4, HGD).sum(0, keepdims=True)
    g_qnw = tot[10, HGW:].reshape(8, HDIM).sum(0, keepdims=True)
    g_knw = tot[11, :128].reshape(2, HDIM).sum(0, keepdims=True)
    g_sinks = tot[16:24, 0][None]
    g_lg = lax.dynamic_slice(tot[12:16, :HGW].reshape(2, 2, HGW), (0, 0, chip * 128), (2, 2, 128))

    names = ["c_ctx", "w_ada", "b_ada", "norm_mix_w", "norm_ffn_w", "w_in", "hgrn_lb_logits", "hgrn_norm_w",
             "q_norm_w", "k_norm_w", "attn_sinks", "w_branch_hgrn", "w_branch_attn", "w_out", "w_ffn_gate",
             "w_ffn_up", "w_ffn_down"]
    ws = dict(zip(names, [c_ctx, w_ada, b_ada, norm_mix_w, norm_ffn_w, w_in, hgrn_lb_logits, hgrn_norm_w,
                          q_norm_w, k_norm_w, attn_sinks, w_branch_hgrn, w_branch_attn, w_out, w_ffn_gate,
                          w_ffn_up, w_ffn_down]))
    ms = dict(zip(names, [m_c_ctx, m_w_ada, m_b_ada, m_norm_mix_w, m_norm_ffn_w, m_w_in, m_hgrn_lb_logits,
                          m_hgrn_norm_w, m_q_norm_w, m_k_norm_w, m_attn_sinks, m_w_branch_hgrn,
                          m_w_branch_attn, m_w_out, m_w_ffn_gate, m_w_ffn_up, m_w_ffn_down]))
    vs = dict(zip(names, [v_c_ctx, v_w_ada, v_b_ada, v_norm_mix_w, v_norm_ffn_w, v_w_in, v_hgrn_lb_logits,
                          v_hgrn_norm_w, v_q_norm_w, v_k_norm_w, v_attn_sinks, v_w_branch_hgrn,
                          v_w_branch_attn, v_w_out, v_w_ffn_gate, v_w_ffn_up, v_w_ffn_down]))
    transposed = ("w_in", "w_ffn_gate", "w_ffn_up")

    def view(a, n):
        return a[0].T if n in transposed else a[0]

    def unview(a, n):
        return a.T[None] if n in transposed else a[None]

    delta, new_m, new_v, grads = {}, {}, {}, {}

    def big_adamw(group, gs, name, carry=None):
        (d_, m_, v_), extra = _adamw_multi([view(ws[n], n) for n in group], gs, [view(ms[n], n) for n in group],
                                           [view(vs[n], n) for n in group], name=name, carry=carry)
        for i, n in enumerate(group):
            grads[n], delta[n], new_m[n], new_v[n] = (unview(gs[i], n), unview(d_[i], n), unview(m_[i], n),
                                                      unview(v_[i], n))
        return extra

    big_adamw(["w_ffn_down", "w_ffn_gate", "w_ffn_up", "w_out", "w_branch_hgrn", "w_branch_attn"],
              [g_dn, g_g, g_u, g_o, g_bh, g_ba], "adamw_first")
    in_reds = _rs_chip_add(in_pairs, in_contribs, ci, chip)
    g_in, = [whole(r) for r in _rs_sibling_gather(in_reds)]
    big_adamw(["w_in", "w_ada"], [g_in, g_w_ada], "adamw_second")
    grads.update(c_ctx=g_c_ctx, b_ada=g_b_ada, norm_mix_w=g_nw1, norm_ffn_w=g_nw2, hgrn_lb_logits=g_lg,
                 hgrn_norm_w=g_hw, q_norm_w=g_qnw, k_norm_w=g_knw, attn_sinks=g_sinks)
    small_names = [n for n in names if n not in delta]

    def two_d(a):
        return a.reshape(1, -1) if a.ndim == 1 else a

    sd, sm_, sv = _adamw_small(*[[two_d(d[n]) for n in small_names] for d in (ws, grads, ms, vs)])
    for i, n in enumerate(small_names):
        for dst, src in ((delta, sd), (new_m, sm_), (new_v, sv)):
            dst[n] = src[i].reshape(ws[n].shape)
    return (loss, gx[None], *[grads[n] for n in names], *[delta[n] for n in names],
            *[new_m[n] for n in names], *[new_v[n] for n in names])
```

```python
import functools

import numpy as np
import jax
import jax.numpy as jnp
from jax import lax
from jax.experimental import pallas as pl
from jax.experimental.pallas import tpu as pltpu

F32 = jnp.float32
BF16 = jnp.bfloat16
HI = lax.Precision.HIGHEST
MESH = pl.DeviceIdType.MESH

D = 1024
L = 256
TM = 256
HGW = 512
HGD = 128
CH = 32
ATW = 512
HDIM = 64
BLK = 128
GRID_W = 64
DFF = 2816
NCOL = 5376
EPS = 1e-6
ROPE_THETA = 10000.0
BF16_SUBLANES = 16

C_FB, C_INP, C_QHG, C_FF = 0, 1, 2, 3
C_GATES = 1
C_GHG, C_QRAW = 8, 9
C_KV = 20
C_QKV = 6

ADAM_LR, ADAM_B1, ADAM_B2, ADAM_EPS, ADAM_WD, ADAM_STEP = 0.001, 0.9, 0.999, 1e-08, 0.01, 10

NN = (((1,), (0,)), ((), ()))
NT = (((1,), (1,)), ((), ()))
TN = (((0,), (0,)), ((), ()))


def _dot(a, b, dims=NN, prec=None):
    return lax.dot_general(a, b, dims, precision=prec, preferred_element_type=F32)


def _bdot(a, b, dims=NN):
    return _dot(a.astype(BF16), b.astype(BF16), dims)


def _sig(x):
    return 1.0 / (1.0 + jnp.exp(-x))


class _Carry:
    def __init__(self, ins, outs, aliases, scratch, phases, peers):
        self.ins, self.outs, self.aliases, self.scratch, self.phases = ins, outs, aliases, scratch, phases
        self.peers = peers


BARRIER_IDS = {"sib": 1, "chips": 2, "both": 3}


def _peer_barrier(kind):
    x, y, c = _place()
    peers = []
    if kind in ("sib", "both"):
        peers.append((x, y, 1 - c))
    if kind in ("chips", "both"):
        peers += [(1 - x, y, c), (x, 1 - y, c), (1 - x, 1 - y, c)]
    bar = pltpu.get_barrier_semaphore()
    for peer in peers:
        pl.semaphore_signal(bar, inc=1, device_id=peer, device_id_type=MESH)
    pl.semaphore_wait(bar, len(peers))


def _in_hbm(args):
    return [pltpu.with_memory_space_constraint(a, pltpu.HBM) for a in args]


def _out_hbm(shapes):
    if isinstance(shapes, (list, tuple)):
        return [pltpu.HBM(s.shape, s.dtype) for s in shapes]
    return pltpu.HBM(shapes.shape, shapes.dtype)


def _carry_join(a, b):
    na_in, na_out, na_sc = len(a.ins), len(a.outs), len(a.scratch)
    aliases = dict(a.aliases)
    aliases.update({na_in + i: na_out + o for i, o in b.aliases.items()})

    def phases(ins, outs, sems):
        pa = a.phases(ins[:na_in], outs[:na_out], sems[:na_sc])
        pb = b.phases(ins[na_in:], outs[na_out:], sems[na_sc:])

        def both(fa, fb):
            if fa is None and fb is None:
                return None

            def run():
                for fn in (fa, fb):
                    if fn is not None:
                        fn()
            return run

        return tuple(both(fa, fb) for fa, fb in zip(pa, pb))

    return _Carry(list(a.ins) + list(b.ins), list(a.outs) + list(b.outs), aliases,
                  list(a.scratch) + list(b.scratch), phases, a.peers if a.peers == b.peers else "both")


def _pcall(body, *, name, grid, in_specs, out_specs, out_shape, scratch=(), aliases=None, vmem_mb=48,
           carry=None):
    params = pltpu.CompilerParams(dimension_semantics=("arbitrary",) * len(grid),
                                  vmem_limit_bytes=vmem_mb << 20)
    if carry is None:
        plain = pl.pallas_call(
            body, name=name, grid=grid, in_specs=in_specs, out_specs=out_specs, out_shape=_out_hbm(out_shape),
            scratch_shapes=list(scratch), input_output_aliases=aliases or {}, compiler_params=params)
        return lambda *args: plain(*_in_hbm(args))
    single = not isinstance(out_shape, (list, tuple))
    out_specs_l = [out_specs] if single else list(out_specs)
    out_shape_l = [out_shape] if single else list(out_shape)
    n_in, n_out, n_sc = len(in_specs), len(out_shape_l), len(scratch)
    k_in, k_out = len(carry.ins), len(carry.outs)
    nsteps = int(np.prod(grid))
    assert nsteps >= 3

    def wrapped(*refs):
        ins, cins = refs[:n_in], refs[n_in:n_in + k_in]
        o0 = n_in + k_in
        outs, couts = refs[o0:o0 + n_out], refs[o0 + n_out:o0 + n_out + k_out]
        s0 = o0 + n_out + k_out
        sc, csc = refs[s0:s0 + n_sc], refs[s0 + n_sc:]
        step = pl.program_id(0)
        for ax in range(1, len(grid)):
            step = step * grid[ax] + pl.program_id(ax)
        start, mid, end = carry.phases(cins, couts, csc)

        @pl.when(step == 0)
        def _():
            _peer_barrier(carry.peers)
            start()

        body(*ins, *outs, *sc)
        if mid is not None:
            pl.when(step == nsteps - 2)(mid)
        pl.when(step == nsteps - 1)(end)

    all_aliases = dict(aliases or {})
    all_aliases.update({n_in + i: n_out + o for i, o in carry.aliases.items()})
    call = pl.pallas_call(
        wrapped, name=name, grid=grid, in_specs=list(in_specs) + [ANY] * k_in,
        out_specs=out_specs_l + [ANY] * k_out, out_shape=_out_hbm(out_shape_l + list(carry.outs)),
        scratch_shapes=list(scratch) + list(carry.scratch), input_output_aliases=all_aliases,
        compiler_params=pltpu.CompilerParams(dimension_semantics=("arbitrary",) * len(grid),
                                             vmem_limit_bytes=vmem_mb << 20,
                                             collective_id=BARRIER_IDS[carry.peers]))

    def run(*args):
        res = call(*_in_hbm(args), *carry.ins)
        core = res[:n_out]
        return (core[0] if single else list(core)), list(res[n_out:])

    return run


def _full(shape):
    nd = len(shape)
    return pl.BlockSpec(shape, lambda *_: (0,) * nd)


ANY = pl.BlockSpec(memory_space=pl.ANY)


NT_IN = NCOL // 256


def _src_block(j):
    return j + jnp.where(j < 4, 2, jnp.where(j < 6, 3, jnp.where(j < 8, -6, jnp.where(
        j < 16, 5, jnp.where(j < 20, -7, -14)))))


def _mm_in(h, wt, tm, carry=None):
    tt = h.shape[0]

    def body(h_ref, w_ref, o_ref):
        o_ref[...] = _bdot(h_ref[...], w_ref[...], NT)

    return _pcall(body, name="mm_in", grid=(tt // tm, NT_IN),
                  in_specs=[pl.BlockSpec((tm, D), lambda i, j: (i, 0)),
                            pl.BlockSpec((256, D), lambda i, j: (_src_block(j), 0))],
                  out_specs=pl.BlockSpec((tm, 256), lambda i, j: (i, j)),
                  out_shape=jax.ShapeDtypeStruct((tt, NCOL), F32), carry=carry)(h, wt)


def _mm_dh(dp, wt, tm, carry=None):
    tt = dp.shape[0]
    per, ng = 3, NT_IN // 3

    def body(d_ref, w0, w1, w2, o_ref, acc):
        kk = pl.program_id(1)

        @pl.when(kk == 0)
        def _():
            acc[...] = jnp.zeros_like(acc)

        acc[...] += (_bdot(d_ref[:, 0:256], w0[...]) + _bdot(d_ref[:, 256:512], w1[...])
                     + _bdot(d_ref[:, 512:768], w2[...]))

        @pl.when(kk == ng - 1)
        def _():
            o_ref[...] = acc[...]

    wspecs = [pl.BlockSpec((256, D), functools.partial(lambda t, i, kk: (_src_block(per * kk + t), 0), t))
              for t in range(per)]
    return _pcall(body, name="mm_dh", grid=(tt // tm, ng),
                  in_specs=[pl.BlockSpec((tm, per * 256), lambda i, kk: (i, kk))] + wspecs,
                  out_specs=pl.BlockSpec((tm, D), lambda i, kk: (i, 0)),
                  out_shape=jax.ShapeDtypeStruct((tt, D), F32), scratch=[pltpu.VMEM((tm, D), F32)],
                  carry=carry)(dp, wt, wt, wt)


def _mm_gin(dp, h, tk):
    tt = dp.shape[0]
    nk = tt // tk

    def body(d_ref, h_ref, o_ref, acc):
        kk = pl.program_id(1)

        @pl.when(kk == 0)
        def _():
            acc[...] = jnp.zeros_like(acc)

        acc[...] += _bdot(d_ref[...], h_ref[...], TN)

        @pl.when(kk == nk - 1)
        def _():
            o_ref[...] = acc[...].astype(BF16)

    return _pcall(body, name="mm_gin", grid=(NT_IN, nk),
                  in_specs=[pl.BlockSpec((tk, 256), lambda j, kk: (kk, j)),
                            pl.BlockSpec((tk, D), lambda j, kk: (kk, 0))],
                  out_specs=pl.BlockSpec((256, D), lambda j, kk: (_src_block(j), 0)),
                  out_shape=jax.ShapeDtypeStruct((NCOL, D), BF16), scratch=[pltpu.VMEM((256, D), F32)])(dp, h)


def _tok_specs():
    assert L == TM
    return [_full((TM, D)), pl.BlockSpec((TM, D), lambda i: (jnp.maximum(i - 1, 0), 0))]


def _mod1(ctx, x, nw, ss):
    rows = L + x.shape[0]

    def body(c_ref, x_ref, nw_ref, ss_ref, h_ref):
        t = jnp.where(pl.program_id(0) == 0, c_ref[...], x_ref[...])
        r = lax.rsqrt(jnp.mean(t * t, axis=-1, keepdims=True) + EPS)
        s = ss_ref[0]
        h_ref[...] = ((t * r * nw_ref[...]) * (1.0 + s[1:2]) + s[0:1]).astype(BF16)

    return _pcall(body, name="mod1", grid=(rows // TM,),
                  in_specs=_tok_specs() + [_full((1, D)),
                                           pl.BlockSpec((1, 2, D), lambda i: (jnp.minimum(i, 1), 0, 0))],
                  out_specs=pl.BlockSpec((TM, D), lambda i: (i, 0)),
                  out_shape=jax.ShapeDtypeStruct((rows, D), BF16))(ctx, x, nw, ss)


def _norm_bwd_rows(x, dh, nw, scale):
    r = lax.rsqrt(jnp.mean(x * x, axis=-1, keepdims=True) + EPS)
    xh = x * r
    dxh = dh * ((1.0 + scale) * nw)
    dx = r * (dxh - xh * jnp.mean(dxh * xh, axis=-1, keepdims=True))
    return dx, xh


def _out_proj_mod2(mixed, w_o, x, g1, nw2, ss2):
    s_len = x.shape[0]
    tm = 512

    def body(m_ref, w_ref, x_ref, g_ref, nw_ref, ss_ref, ao_ref, x1_ref, h_ref):
        ao = _bdot(m_ref[...], w_ref[...])
        ao_ref[...] = ao.astype(BF16)
        x1 = x_ref[...] + g_ref[...] * ao
        x1_ref[...] = x1
        r = lax.rsqrt(jnp.mean(x1 * x1, axis=-1, keepdims=True) + EPS)
        s = ss_ref[0]
        h_ref[...] = ((x1 * r * nw_ref[...]) * (1.0 + s[1:2]) + s[0:1]).astype(BF16)

    row = pl.BlockSpec((tm, D), lambda i: (i, 0))
    f = jax.ShapeDtypeStruct((s_len, D), F32)
    return _pcall(body, name="out_proj_mod2", grid=(s_len // tm,),
                  in_specs=[row, _full((D, D)), row, _full((1, D)), _full((1, D)), _full((1, 2, D))],
                  out_specs=[row, row, row],
                  out_shape=[jax.ShapeDtypeStruct((s_len, D), BF16), f,
                             jax.ShapeDtypeStruct((s_len, D), BF16)])(mixed, w_o, x, g1, nw2, ss2)


TS = 1024


def _acc_call(body, *, name, grid, in_specs, out_specs, out_shape, acc_shapes, args, carry=None):
    return _pcall(body, name=name, grid=grid, in_specs=in_specs, out_specs=out_specs, out_shape=out_shape,
                  scratch=[pltpu.VMEM(s, F32) for s in acc_shapes], carry=carry)(*args)


def _ffn_up(h2, g4, u4, carry=None):
    s_len = h2.shape[0]
    ns = g4.shape[1]

    def body(h_ref, g_ref, u_ref, a_ref, b_ref, z_ref):
        h = h_ref[...]
        a = _bdot(h, g_ref[0], NT)
        b = _bdot(h, u_ref[0], NT)
        a_ref[0] = a.astype(BF16)
        b_ref[0] = b.astype(BF16)
        z_ref[0] = (a * _sig(a) * b).astype(BF16)

    w = pl.BlockSpec((1, ns, D), lambda i, j: (j, 0, 0))
    o = pl.BlockSpec((1, TS, ns), lambda i, j: (j, i, 0))
    f = jax.ShapeDtypeStruct((4, s_len, ns), BF16)
    return _pcall(body, name="ffn_up", grid=(s_len // TS, 4),
                  in_specs=[pl.BlockSpec((TS, D), lambda i, j: (i, 0)), w, w], out_specs=[o, o, o],
                  out_shape=[f, f, jax.ShapeDtypeStruct((4, s_len, ns), BF16)], carry=carry)(h2, g4, u4)


def _ffn_down_loss(z4, dn4, x1, g2, tgt):
    _, s_len, ns = z4.shape

    def body(z_ref, w_ref, x1_ref, g_ref, t_ref, sq_ref, dx2_ref, dyb_ref, dg_ref, acc):
        i, j = pl.program_id(0), pl.program_id(1)

        @pl.when((i == 0) & (j == 0))
        def _():
            sq_ref[...] = jnp.zeros_like(sq_ref)
            dg_ref[...] = jnp.zeros_like(dg_ref)

        @pl.when(j == 0)
        def _():
            acc[...] = jnp.zeros_like(acc)

        acc[...] += _bdot(z_ref[0], w_ref[0])

        @pl.when(j == 3)
        def _():
            y_ = acc[...]
            g = g_ref[...]
            e = x1_ref[...] + g * y_ - t_ref[...]
            sq_ref[...] += jnp.sum(e * e, axis=0, keepdims=True)
            dx2 = e * (1.0 / D)
            dx2_ref[...] = dx2
            dyb_ref[...] = (g * dx2).astype(BF16)
            dg_ref[...] += jnp.sum(dx2 * y_, axis=0, keepdims=True)

    row = pl.BlockSpec((TS, D), lambda i, j: (i, 0))
    vec = _full((1, D))
    return _acc_call(body, name="ffn_down_loss", grid=(s_len // TS, 4),
                     in_specs=[pl.BlockSpec((1, TS, ns), lambda i, j: (j, i, 0)),
                               pl.BlockSpec((1, ns, D), lambda i, j: (j, 0, 0)), row, vec, row],
                     out_specs=[vec, row, row, vec],
                     out_shape=[jax.ShapeDtypeStruct((1, D), F32), jax.ShapeDtypeStruct((s_len, D), F32),
                                jax.ShapeDtypeStruct((s_len, D), BF16), jax.ShapeDtypeStruct((1, D), F32)],
                     acc_shapes=[(TS, D)], args=(z4, dn4, x1, g2, tgt))


def _ffn_dz(dyb, dn4, a4, b4):
    _, s_len, ns = a4.shape

    def body(dy_ref, w_ref, a_ref, b_ref, da_ref, db_ref):
        dz = _bdot(dy_ref[...], w_ref[0], NT)
        a = a_ref[0].astype(F32)
        s = _sig(a)
        da_ref[0] = (dz * b_ref[0].astype(F32) * (s * (1.0 + a * (1.0 - s)))).astype(BF16)
        db_ref[0] = (dz * (a * s)).astype(BF16)

    t = pl.BlockSpec((1, TS, ns), lambda i, j: (j, i, 0))
    o = jax.ShapeDtypeStruct((4, s_len, ns), BF16)
    return _pcall(body, name="ffn_dz", grid=(s_len // TS, 4),
                  in_specs=[pl.BlockSpec((TS, D), lambda i, j: (i, 0)),
                            pl.BlockSpec((1, ns, D), lambda i, j: (j, 0, 0)), t, t],
                  out_specs=[t, t], out_shape=[o, o])(dyb, dn4, a4, b4)


def _ffn_gdn(z4, dyb):
    _, s_len, ns = z4.shape
    tk = min(s_len, 2 * TS)
    nk = s_len // tk

    def body(z_ref, dy_ref, o_ref, acc):
        t = pl.program_id(1)

        @pl.when(t == 0)
        def _():
            acc[...] = jnp.zeros_like(acc)

        acc[...] += _bdot(z_ref[0], dy_ref[...], TN)

        @pl.when(t == nk - 1)
        def _():
            o_ref[0] = acc[...].astype(o_ref.dtype)

    return _acc_call(body, name="ffn_gdn", grid=(4, nk),
                     in_specs=[pl.BlockSpec((1, tk, ns), lambda j, t: (j, t, 0)),
                               pl.BlockSpec((tk, D), lambda j, t: (t, 0))],
                     out_specs=pl.BlockSpec((1, ns, D), lambda j, t: (j, 0, 0)),
                     out_shape=jax.ShapeDtypeStruct((4, ns, D), BF16), acc_shapes=[(ns, D)], args=(z4, dyb))


def _ffn_dh2(da4, db4, g4, u4, carry=None):
    _, s_len, ns = da4.shape

    def body(da_ref, db_ref, g_ref, u_ref, o_ref, acc):
        j = pl.program_id(1)

        @pl.when(j == 0)
        def _():
            acc[...] = jnp.zeros_like(acc)

        acc[...] += _bdot(da_ref[0], g_ref[0]) + _bdot(db_ref[0], u_ref[0])

        @pl.when(j == 3)
        def _():
            o_ref[...] = acc[...]

    t = pl.BlockSpec((1, TS, ns), lambda i, j: (j, i, 0))
    w = pl.BlockSpec((1, ns, D), lambda i, j: (j, 0, 0))
    return _acc_call(body, name="ffn_dh2", grid=(s_len // TS, 4), in_specs=[t, t, w, w],
                     out_specs=pl.BlockSpec((TS, D), lambda i, j: (i, 0)),
                     out_shape=jax.ShapeDtypeStruct((s_len, D), F32), acc_shapes=[(TS, D)],
                     args=(da4, db4, g4, u4), carry=carry)


def _ffn_ggu(h2, da4, db4, carry=None):
    _, s_len, ns = da4.shape
    nk = s_len // TS

    def body(h_ref, da_ref, db_ref, gg_ref, gu_ref, acc_g, acc_u):
        t = pl.program_id(1)

        @pl.when(t == 0)
        def _():
            acc_g[...] = jnp.zeros_like(acc_g)
            acc_u[...] = jnp.zeros_like(acc_u)

        h = h_ref[...]
        acc_g[...] += _bdot(da_ref[0], h, TN)
        acc_u[...] += _bdot(db_ref[0], h, TN)

        @pl.when(t == nk - 1)
        def _():
            gg_ref[0] = acc_g[...].astype(BF16)
            gu_ref[0] = acc_u[...].astype(BF16)

    d = pl.BlockSpec((1, TS, ns), lambda j, t: (j, t, 0))
    o = pl.BlockSpec((1, ns, D), lambda j, t: (j, 0, 0))
    f = jax.ShapeDtypeStruct((4, ns, D), BF16)
    return _acc_call(body, name="ffn_ggu", grid=(4, nk),
                     in_specs=[pl.BlockSpec((TS, D), lambda j, t: (t, 0)), d, d], out_specs=[o, o],
                     out_shape=[f, f], acc_shapes=[(ns, D), (ns, D)], args=(h2, da4, db4), carry=carry)


def _mod2_bwd(x1, dh2, dx2, ao, nw2, ss2, g1):
    s_len = x1.shape[0]

    def body(x1_ref, dh_ref, dx2_ref, ao_ref, nw_ref, ss_ref, g_ref,
             dx1_ref, da_ref, dss_ref, dnw_ref, dg_ref):
        i = pl.program_id(0)

        @pl.when(i == 0)
        def _():
            dss_ref[...] = jnp.zeros_like(dss_ref)
            dnw_ref[...] = jnp.zeros_like(dnw_ref)
            dg_ref[...] = jnp.zeros_like(dg_ref)

        dh = dh_ref[...]
        nw = nw_ref[...]
        scale = ss_ref[0][1:2]
        dxn, xh = _norm_bwd_rows(x1_ref[...], dh, nw, scale)
        dx1 = dx2_ref[...] + dxn
        dx1_ref[...] = dx1
        da_ref[...] = (g_ref[...] * dx1).astype(BF16)
        dg_ref[...] += jnp.sum(dx1 * ao_ref[...].astype(F32), axis=0, keepdims=True)
        dsh = jnp.sum(dh, axis=0, keepdims=True)
        dsc = jnp.sum(dh * xh * nw, axis=0, keepdims=True)
        dss_ref[...] += jnp.concatenate([dsh, dsc], axis=0)
        dnw_ref[...] += jnp.sum(dh * xh * (1.0 + scale), axis=0, keepdims=True)

    row = pl.BlockSpec((TM, D), lambda i: (i, 0))
    vec = _full((1, D))
    return _pcall(body, name="mod2_bwd", grid=(s_len // TM,),
                  in_specs=[row, row, row, row, vec, _full((1, 2, D)), vec],
                  out_specs=[row, row, _full((2, D)), vec, vec],
                  out_shape=[jax.ShapeDtypeStruct((s_len, D), F32), jax.ShapeDtypeStruct((s_len, D), BF16),
                             jax.ShapeDtypeStruct((2, D), F32), jax.ShapeDtypeStruct((1, D), F32),
                             jax.ShapeDtypeStruct((1, D), F32)])(x1, dh2, dx2, ao, nw2, ss2, g1)


def _mod1_bwd(ctx, x, dh, dx1, nw1, ss1):
    s_len = dx1.shape[0]
    tt = L + s_len

    def body(c_ref, x_ref, dh_ref, dx1_ref, nw_ref, ss_ref, dx_ref, dss_ref, dnw_ref):
        i = pl.program_id(0)
        tok = jnp.where(i == 0, c_ref[...], x_ref[...])

        @pl.when(i == 0)
        def _():
            dnw_ref[...] = jnp.zeros_like(dnw_ref)

        @pl.when(i <= 1)
        def _():
            dss_ref[...] = jnp.zeros_like(dss_ref)

        dh_ = dh_ref[...]
        nw = nw_ref[...]
        scale = ss_ref[0][1:2]
        dxn, xh = _norm_bwd_rows(tok, dh_, nw, scale)

        @pl.when(i >= 1)
        def _():
            dx_ref[...] = dx1_ref[...] + dxn

        dsh = jnp.sum(dh_, axis=0, keepdims=True)
        dsc = jnp.sum(dh_ * xh * nw, axis=0, keepdims=True)
        dss_ref[...] += jnp.concatenate([dsh, dsc], axis=0)[None]
        dnw_ref[...] += jnp.sum(dh_ * xh * (1.0 + scale), axis=0, keepdims=True)

    row = pl.BlockSpec((TM, D), lambda i: (i, 0))
    lat = pl.BlockSpec((TM, D), lambda i: (jnp.maximum(i - 1, 0), 0))
    sel = pl.BlockSpec((1, 2, D), lambda i: (jnp.minimum(i, 1), 0, 0))
    return _pcall(body, name="mod1_bwd", grid=(tt // TM,),
                  in_specs=_tok_specs() + [row, lat, _full((1, D)), sel],
                  out_specs=[lat, sel, _full((1, D))],
                  out_shape=[jax.ShapeDtypeStruct((s_len, D), F32), jax.ShapeDtypeStruct((2, 2, D), F32),
                             jax.ShapeDtypeStruct((1, D), F32)])(ctx, x, dh, dx1, nw1, ss1)


def _rows(c):
    return slice(c * CH, (c + 1) * CH)


def _chunk_masks(rev, transpose=False):
    r = lax.broadcasted_iota(jnp.int32, (TM, TM), 0)
    c = lax.broadcasted_iota(jnp.int32, (TM, TM), 1)
    same = (r // CH) == (c // CH)
    before = (c >= r) if (rev != transpose) else (c <= r)
    return same & before, same


def _chunk_scan(x, rev, transpose=False):
    r = lax.broadcasted_iota(jnp.int32, (CH, CH), 0)
    c = lax.broadcasted_iota(jnp.int32, (CH, CH), 1)
    tri = ((c >= r) if (rev != transpose) else (c <= r)).astype(F32)
    return jnp.concatenate([_dot(tri, x[_rows(ch)], prec=HI) for ch in range(x.shape[0] // CH)], axis=0)


def _chunk_total(x):
    return jnp.concatenate([jnp.broadcast_to(jnp.sum(x[_rows(ch)], axis=0, keepdims=True), (CH, x.shape[1]))
                            for ch in range(x.shape[0] // CH)], axis=0)


def _hgrn_gate(fl, qraw, lg):
    lb = 1.0 / (1.0 + jnp.exp(lg[1:2] - lg[0:1]))
    sg = _sig(fl)
    f = lb + (1.0 - lb) * sg
    q = qraw * _sig(qraw) * (HGD ** -0.5)
    return lb, sg, f, q


def _hgrn_fwd(p, lg, *, rev, carry=None, readout=None):
    tt = p.shape[0]
    nt = tt // TM
    ncht = TM // CH
    d = 1 if rev else 0

    def tile_of(s):
        return jnp.where(s == 0, 0, nt - s) if rev else s

    def body(*refs):
        if readout is None:
            f_ref, inp_ref, q_ref, lg_ref, o_ref, st_ref, state = refs
        else:
            f_ref, inp_ref, q_ref, lg_ref, oo_ref, g_ref, hw_ref, o_ref, st_ref, y_ref, state = refs
        s = pl.program_id(0)

        @pl.when(s == 0)
        def _():
            state[...] = jnp.zeros_like(state)

        _, _, f, q = _hgrn_gate(f_ref[...], q_ref[...], lg_ref[0])
        lf = jnp.log(f)
        causal, _ = _chunk_masks(rev)
        cum = _chunk_scan(lf, rev)
        tot = _chunk_total(lf)
        qd = (q * jnp.exp(cum)).astype(BF16)
        kd = ((1.0 - f) * jnp.exp(-cum)).astype(BF16)
        ke = ((1.0 - f) * jnp.exp(tot - cum)).astype(BF16)
        et = jnp.exp(tot)
        v = inp_ref[...].astype(BF16)
        order = range(ncht - 1, -1, -1) if rev else range(ncht)
        outs = []
        for h in range(4):
            sl = slice(h * HGD, (h + 1) * HGD)
            qd_, kd_, ke_, v_ = qd[:, sl], kd[:, sl], ke[:, sl], v[:, sl]
            pm = jnp.where(causal, _dot(qd_, kd_, NT), 0.0).astype(BF16)
            o_h = _dot(pm, v_)
            upd = [_dot(v_[_rows(c)], ke_[_rows(c)], TN) for c in range(ncht)]
            st = state[h]
            for c in order:
                st_ref[c, h] = st
                st = st * et[c * CH:c * CH + 1, sl] + upd[c]
            state[h] = st
            inter = [_dot(qd_[_rows(c)], st_ref[c, h].astype(BF16), NT) for c in range(ncht)]
            outs.append(o_h + jnp.concatenate(inter, axis=0))
        o_tile = jnp.concatenate(outs, axis=1)
        o_ref[...] = o_tile
        if readout is not None:
            @pl.when(tile_of(s) >= 1)
            def _():
                g = g_ref[...]
                y_ref[...] = (_head_rms(oo_ref[...] + o_tile, None, 4) * hw_ref[...] * (g * _sig(g))).astype(BF16)

    def col(cb):
        return pl.BlockSpec((TM, HGW), lambda s: (tile_of(s), cb))

    in_specs = [col(C_FB if rev else C_FF), col(C_INP), col(C_QHG), pl.BlockSpec((1, 2, HGW), lambda s: (d, 0, 0))]
    out_specs = [col(0), pl.BlockSpec((ncht, 4, HGD, HGD), lambda s: (tile_of(s), 0, 0, 0))]
    out_shape = [jax.ShapeDtypeStruct((tt, HGW), F32), jax.ShapeDtypeStruct((nt * ncht, 4, HGD, HGD), F32)]
    args = [p, p, p, lg]
    if readout is not None:
        in_specs += [col(0), col(C_GHG), _full((1, HGW))]
        args += [readout[0], p, readout[1]]
        assert rev
        out_specs.append(pl.BlockSpec((TM, HGW), lambda s: (jnp.where(s == 0, nt - 2, tile_of(s) - 1), 0)))
        out_shape.append(jax.ShapeDtypeStruct((tt - L, HGW), BF16))
    return _pcall(body, name="hgrn_fwd_rev" if rev else "hgrn_fwd", grid=(nt,), in_specs=in_specs,
                  out_specs=out_specs, out_shape=out_shape, scratch=[pltpu.VMEM((4, HGD, HGD), F32)],
                  carry=carry)(*args)


def _hgrn_bwd(p, lg, do, st, dp, prev, *, rev, carry=None):
    tt = p.shape[0]
    nt = tt // TM
    ncht = TM // CH
    d = 1 if rev else 0
    second = prev is not None

    def tile_of(s):
        return jnp.where(s == nt - 1, 0, s + 1) if rev else nt - 1 - s

    def body(*refs):
        if second:
            (f_ref, inp_ref, q_ref, lg_ref, do_ref, st_ref, dvp_ref, dqp_ref, _dp_in,
             dp_ref, dlg_ref, dstate) = refs
        else:
            (f_ref, inp_ref, q_ref, lg_ref, do_ref, st_ref, _dp_in,
             dp_ref, dv_ref, dq_ref, dlg_ref, dstate) = refs
        s = pl.program_id(0)
        tile = tile_of(s)

        @pl.when(s == 0)
        def _():
            dstate[...] = jnp.zeros_like(dstate)
            dlg_ref[...] = jnp.zeros_like(dlg_ref)

        qraw = q_ref[...]
        lb, sg, f, q = _hgrn_gate(f_ref[...], qraw, lg_ref[0])
        lf = jnp.log(f)
        causal, _ = _chunk_masks(rev)
        causal_t, _ = _chunk_masks(rev, transpose=True)
        cum = _chunk_scan(lf, rev)
        tot = _chunk_total(lf)
        ea, eb, ee, et = jnp.exp(cum), jnp.exp(-cum), jnp.exp(tot - cum), jnp.exp(tot)
        qdf, kdf, kef = q * ea, (1.0 - f) * eb, (1.0 - f) * ee
        qd, kd, ke = qdf.astype(BF16), kdf.astype(BF16), kef.astype(BF16)
        v = inp_ref[...].astype(BF16)
        dob = jnp.where(tile == 0, 0.0, do_ref[...]).astype(BF16)
        order = range(ncht) if rev else range(ncht - 1, -1, -1)
        dq_l, dk_l, dv_l, dcum_l, dtot_l = [], [], [], [], []
        for h in range(4):
            sl = slice(h * HGD, (h + 1) * HGD)
            qd_, kd_, ke_, v_, do_ = qd[:, sl], kd[:, sl], ke[:, sl], v[:, sl], dob[:, sl]
            pmt = jnp.where(causal_t, _dot(kd_, qd_, NT), 0.0).astype(BF16)
            dpm = jnp.where(causal, _dot(do_, v_, NT), 0.0).astype(BF16)
            dpmt = jnp.where(causal_t, _dot(v_, do_, NT), 0.0).astype(BF16)
            dv = _dot(pmt, do_)
            dqd = _dot(dpm, kd_)
            dkd = _dot(dpmt, qd_)
            upd = [_dot(do_[_rows(c)], qd_[_rows(c)], TN) for c in range(ncht)]
            ds = dstate[h]
            ds1 = [None] * ncht
            for c in order:
                ds1[c] = ds
                ds = ds * et[c * CH:c * CH + 1, sl] + upd[c]
            dstate[h] = ds
            dke_c, dv_c, dqd_c, dtot_c = [], [], [], []
            for c in range(ncht):
                st0 = st_ref[c, h]
                dsb = ds1[c].astype(BF16)
                dke_ = _dot(v_[_rows(c)], dsb)
                dke_c.append(dke_)
                dv_c.append(_dot(ke_[_rows(c)], dsb, NT))
                dqd_c.append(_dot(do_[_rows(c)], st0.astype(BF16)))
                dt = (jnp.sum(ds1[c] * st0, axis=0, keepdims=True) * et[c * CH:c * CH + 1, sl]
                      + jnp.sum(dke_ * kef[_rows(c), sl], axis=0, keepdims=True))
                dtot_c.append(jnp.broadcast_to(dt, (CH, HGD)))
            dke = jnp.concatenate(dke_c, axis=0)
            dqd = dqd + jnp.concatenate(dqd_c, axis=0)
            dv_l.append(dv + jnp.concatenate(dv_c, axis=0))
            dtot_l.append(jnp.concatenate(dtot_c, axis=0))
            dq_l.append(dqd * ea[:, sl])
            dk_l.append(dkd * eb[:, sl] + dke * ee[:, sl])
            dcum_l.append(dqd * qdf[:, sl] - dkd * kdf[:, sl] - dke * kef[:, sl])
        dcum = jnp.concatenate(dcum_l, axis=1)
        dlf = _chunk_scan(dcum, rev, transpose=True) + jnp.concatenate(dtot_l, axis=1)
        dq_t = jnp.concatenate(dq_l, axis=1)
        dv_t = jnp.concatenate(dv_l, axis=1)

        df = dlf / f - jnp.concatenate(dk_l, axis=1)
        dfl = df * (1.0 - lb) * sg * (1.0 - sg)
        dlb = jnp.sum(df * (1.0 - sg), axis=0, keepdims=True)
        dl0 = dlb * lb * (1.0 - lb)
        dlg_ref[...] += jnp.concatenate([dl0, -dl0], axis=0)[None]
        if second:
            sq = _sig(qraw)
            dqr = (dqp_ref[...] + dq_t) * (HGD ** -0.5) * (sq * (1.0 + qraw * (1.0 - sq)))
            dp_ref[...] = jnp.concatenate([dfl, dvp_ref[...] + dv_t, dqr], axis=1).astype(BF16)
        else:
            dp_ref[...] = dfl.astype(BF16)
            dv_ref[...] = dv_t
            dq_ref[...] = dq_t

    def col(cb):
        return pl.BlockSpec((TM, HGW), lambda s: (tile_of(s), cb))

    tok = pl.BlockSpec((TM, HGW), lambda s: (tile_of(s), 0))
    in_specs = [col(C_FB if rev else C_FF), col(C_INP), col(C_QHG),
                pl.BlockSpec((1, 2, HGW), lambda s: (d, 0, 0)),
                pl.BlockSpec((TM, HGW), lambda s: (jnp.maximum(tile_of(s) - 1, 0), 0)),
                pl.BlockSpec((ncht, 4, HGD, HGD), lambda s: (tile_of(s), 0, 0, 0))]
    args = [p, p, p, lg, do, st]
    dlg_spec = _full((1, 2, HGW))
    dlg_shape = jax.ShapeDtypeStruct((1, 2, HGW), F32)
    if second:
        in_specs += [tok, tok]
        args += [prev[0], prev[1]]
        out_specs = [pl.BlockSpec((TM, 3 * HGW), lambda s: (tile_of(s), 0)), dlg_spec]
        out_shape = [jax.ShapeDtypeStruct(dp.shape, BF16), dlg_shape]
    else:
        out_specs = [pl.BlockSpec((TM, HGW), lambda s: (tile_of(s), C_FB if rev else C_FF)), tok, tok, dlg_spec]
        out_shape = [jax.ShapeDtypeStruct(dp.shape, BF16), jax.ShapeDtypeStruct((tt, HGW), F32),
                     jax.ShapeDtypeStruct((tt, HGW), F32), dlg_shape]
    in_specs.append(ANY)
    args.append(dp)
    return _pcall(body, name="hgrn_bwd_rev" if rev else "hgrn_bwd", grid=(nt,),
                  in_specs=in_specs, out_specs=out_specs, out_shape=out_shape,
                  scratch=[pltpu.VMEM((4, HGD, HGD), F32)],
                  aliases={len(args) - 1: 0}, carry=carry)(*args)


def _head_rms(o, w, nheads):
    outs = []
    for h in range(nheads):
        oh = o[:, h * HGD:(h + 1) * HGD]
        outs.append(oh * lax.rsqrt(jnp.mean(oh * oh, axis=-1, keepdims=True) + EPS))
    return jnp.concatenate(outs, axis=1)


def _readout_bwd(o0, o1, p, hw4, dy, dp, carry=None):
    tt = o0.shape[0]
    s_len = tt - L

    def body(o0_ref, o1_ref, g_ref, w_ref, dy_ref, _dp_in, dp_ref, do_ref, dw_ref):
        i = pl.program_id(0)

        @pl.when(i == 0)
        def _():
            dw_ref[...] = jnp.zeros_like(dw_ref)
            dp_ref[...] = jnp.zeros_like(dp_ref)

        @pl.when(i >= 1)
        def _():
            o = o0_ref[...] + o1_ref[...]
            g = g_ref[...]
            w = w_ref[...]
            sg = _sig(g)
            dy_ = dy_ref[...]
            dsw = dy_ * (g * sg)
            outs, xhs = [], []
            for h in range(4):
                sl = slice(h * HGD, (h + 1) * HGD)
                oh = o[:, sl]
                r = lax.rsqrt(jnp.mean(oh * oh, axis=-1, keepdims=True) + EPS)
                xh = oh * r
                dxh = dsw[:, sl] * w[:, sl]
                outs.append(r * (dxh - xh * jnp.mean(dxh * xh, axis=-1, keepdims=True)))
                xhs.append(xh)
            xh = jnp.concatenate(xhs, axis=1)
            do_ref[...] = jnp.concatenate(outs, axis=1)
            dp_ref[...] = (dy_ * xh * w * (sg * (1.0 + g * (1.0 - sg)))).astype(BF16)
            dw_ref[...] += jnp.sum(dsw * xh, axis=0, keepdims=True)

    tok = pl.BlockSpec((TM, HGW), lambda i: (i, 0))
    lat = pl.BlockSpec((TM, HGW), lambda i: (jnp.maximum(i - 1, 0), 0))
    return _pcall(body, name="readout_bwd", grid=(tt // TM,),
                  in_specs=[tok, tok, pl.BlockSpec((TM, HGW), lambda i: (i, C_GHG)), _full((1, HGW)), lat, ANY],
                  out_specs=[pl.BlockSpec((TM, HGW), lambda i: (i, C_GHG)), lat, _full((1, HGW))],
                  out_shape=[jax.ShapeDtypeStruct(dp.shape, BF16), jax.ShapeDtypeStruct((s_len, HGW), F32),
                             jax.ShapeDtypeStruct((1, HGW), F32)],
                  aliases={5: 0}, carry=carry)(o0, o1, p, hw4, dy, dp)


def _rope_tables(s_len):
    t = np.arange(s_len)
    inv = ROPE_THETA ** (-np.arange(0, 32, 2, dtype=np.float64) / 32)
    def half(pos):
        ang = pos[:, None].astype(np.float64) * inv[None, :]
        return (np.concatenate([np.cos(ang), np.cos(ang)], 1), np.concatenate([-np.sin(ang), np.sin(ang)], 1))
    cr, sr = half(t // GRID_W)
    cc, sc = half(t % GRID_W)
    cos = np.concatenate([cr, cc, cr, cc], 1)
    sin = np.concatenate([sr, sc, sr, sc], 1)
    cos = np.concatenate([np.ones((L, 128)), cos], 0)
    sin = np.concatenate([np.zeros((L, 128)), sin], 0)
    return jnp.asarray(cos, F32), jnp.asarray(sin, F32)


def _blockdiag(n, w):
    i = np.arange(n)
    return jnp.asarray((i[:, None] // w == i[None, :] // w) / float(w), F32)


def _dup_matrix():
    m = np.zeros((128, 512), np.float32)
    for g in range(2):
        for j in range(4):
            for dd in range(HDIM):
                m[64 * g + dd, 256 * g + 64 * j + dd] = 1.0
    return m


def _head_mean(x, blockdiag):
    return _dot(x, blockdiag, prec=lax.Precision.HIGH)


def _rot(x):
    n = x.shape[1]
    lane = lax.broadcasted_iota(jnp.int32, x.shape, 1)
    return jnp.where((lane % 32) < 16, pltpu.roll(x, n - 16, 1), pltpu.roll(x, 16, 1))


def _qk_prep(p, cos, sin, qnw8, knw2, bd512, bd128, dup):
    tt = p.shape[0]

    def body(q_ref, kv_ref, cos_ref, sin_ref, qw_ref, kw_ref, b5_ref, b1_ref, dup_ref,
             qr_ref, k4_ref, v4_ref):
        cos_, sin_ = cos_ref[...], sin_ref[...]
        q = q_ref[...]
        qn = q * lax.rsqrt(_head_mean(q * q, b5_ref[...]) + EPS) * qw_ref[...]
        cos4 = jnp.concatenate([cos_] * 4, axis=1)
        sin4 = jnp.concatenate([sin_] * 4, axis=1)
        qr_ref[...] = ((qn * cos4 + _rot(qn) * sin4) * (HDIM ** -0.5)).astype(BF16)
        kv = kv_ref[...]
        k, v = kv[:, :128], kv[:, 128:]
        kn = k * lax.rsqrt(_head_mean(k * k, b1_ref[...]) + EPS) * kw_ref[...]
        kr = kn * cos_ + _rot(kn) * sin_
        k4_ref[...] = _bdot(kr, dup_ref[...]).astype(BF16)
        v4_ref[...] = _bdot(v, dup_ref[...]).astype(BF16)

    row = lambda w, cb: pl.BlockSpec((TM, w), lambda i: (i, cb))
    out = jax.ShapeDtypeStruct((tt, ATW), BF16)
    return _pcall(body, name="qk_prep", grid=(tt // TM,),
                  in_specs=[row(ATW, C_QRAW), row(256, C_KV), row(128, 0), row(128, 0),
                            _full((1, ATW)), _full((1, 128)), _full((ATW, ATW)), _full((128, 128)),
                            _full((128, ATW))],
                  out_specs=[row(ATW, 0)] * 3, out_shape=[out] * 3)(
                      p, p, cos, sin, qnw8, knw2, bd512, bd128, dup)


def _attn_masks(i, nb):
    r = lax.broadcasted_iota(jnp.int32, (4 * BLK, 3 * BLK + L), 0) % BLK
    c = lax.broadcasted_iota(jnp.int32, (4 * BLK, 3 * BLK + L), 1)
    kpos = (i - 1) * BLK + c
    loc = (jnp.abs(c - BLK - r) <= BLK) & (kpos >= 0) & (kpos < nb * BLK)
    return loc | (c >= 3 * BLK)


def _stack_mask():
    r = lax.broadcasted_iota(jnp.int32, (4 * BLK, 256), 0)
    lane = lax.broadcasted_iota(jnp.int32, (4 * BLK, 256), 1)
    return (r // BLK) == (lane // HDIM)


def _stack_heads(xg, fill=0.0):
    x4 = jnp.concatenate([xg] * 4, axis=0)
    return jnp.where(_stack_mask(), x4, jnp.full_like(x4, fill))


def _unstack_heads(x4):
    out = jnp.where(_lane_mask(0), x4[0:BLK], 0.0)
    for j in range(1, 4):
        out = out + jnp.where(_lane_mask(j), x4[j * BLK:(j + 1) * BLK], 0.0)
    return out


def _per_head_rows(vals):
    return jnp.concatenate([jnp.broadcast_to(v, (BLK, 1)) for v in vals], axis=0)


def _lane_mask(j):
    lane = lax.broadcasted_iota(jnp.int32, (1, 256), 1)
    return (lane // HDIM) == j


def _attn_specs(nb):
    blk = lambda off: pl.BlockSpec((BLK, ATW), lambda i: (jnp.clip(i + off, 0, nb - 1) + 2, 0))
    ctx = pl.BlockSpec((L, ATW), lambda i: (0, 0))
    return blk, ctx


def _attn_fwd(qr, k4, v4, sinks, carry=None):
    tt = qr.shape[0]
    s_len = tt - L
    nb = s_len // BLK

    def body(sk_ref, q_ref, kp, ko, kn, kc, vp, vo, vn, vc, y_ref, lse_ref):
        i = pl.program_id(0)
        valid = _attn_masks(i, nb)
        q = q_ref[...]
        ys, lses = [], []
        for g in range(2):
            gs = slice(256 * g, 256 * g + 256)
            kcat = jnp.concatenate([kp[:, gs], ko[:, gs], kn[:, gs], kc[:, gs]], axis=0)
            vcat = jnp.concatenate([vp[:, gs], vo[:, gs], vn[:, gs], vc[:, gs]], axis=0)
            sink4 = _per_head_rows([sk_ref[4 * g + j] for j in range(4)])
            q4 = _stack_heads(q[:, gs])
            o_parts, l_parts = [], []
            for hp in range(2):
                rows = slice(2 * BLK * hp, 2 * BLK * (hp + 1))
                sink = sink4[rows]
                s = jnp.where(valid[rows], _dot(q4[rows], kcat, NT), -1e30)
                m = jnp.maximum(jnp.max(s, axis=-1, keepdims=True), sink)
                e = jnp.exp(s - m)
                den = jnp.sum(e, axis=-1, keepdims=True) + jnp.exp(sink - m)
                o_parts.append(_bdot(e * (1.0 / den), vcat))
                l_parts.append(jnp.broadcast_to(m + jnp.log(den), (2 * BLK, 256)))
            ys.append(_unstack_heads(jnp.concatenate(o_parts, axis=0)))
            lses.append(_unstack_heads(jnp.concatenate(l_parts, axis=0)))
        y_ref[...] = jnp.concatenate(ys, axis=1).astype(BF16)
        lse_ref[...] = jnp.concatenate(lses, axis=1)

    blk, ctx = _attn_specs(nb)
    out = pl.BlockSpec((BLK, ATW), lambda i: (i, 0))
    return _pcall(body, name="attn_fwd", grid=(nb,),
                  in_specs=[pl.BlockSpec(memory_space=pltpu.SMEM), blk(0),
                            blk(-1), blk(0), blk(1), ctx, blk(-1), blk(0), blk(1), ctx],
                  out_specs=[out, out],
                  out_shape=[jax.ShapeDtypeStruct((s_len, ATW), BF16),
                             jax.ShapeDtypeStruct((s_len, ATW), F32)], carry=carry)(
                      sinks, qr, k4, k4, k4, k4, v4, v4, v4, v4)


def _attn_bwd(qr, k4, v4, sinks, y, lse, dy, carry=None):
    tt = qr.shape[0]
    s_len = tt - L
    nb = s_len // BLK

    def body(sk_ref, q_ref, kp, ko, kn, kc, vp, vo, vn, vc, y_ref, lse_ref, dy_ref,
             dq_ref, dkw_ref, dvw_ref, dkc_ref, dvc_ref, dsk_ref):
        i = pl.program_id(0)

        @pl.when(i == 0)
        def _():
            dkc_ref[...] = jnp.zeros_like(dkc_ref)
            dvc_ref[...] = jnp.zeros_like(dvc_ref)
            dsk_ref[...] = jnp.zeros_like(dsk_ref)

        valid = _attn_masks(i, nb)
        q = q_ref[...]
        dy_ = dy_ref[...]
        dly = dy_ * y_ref[...].astype(F32)
        lse_ = lse_ref[...]
        dqs = []
        for g in range(2):
            gs = slice(256 * g, 256 * g + 256)
            kcat = jnp.concatenate([kp[:, gs], ko[:, gs], kn[:, gs], kc[:, gs]], axis=0)
            vcat = jnp.concatenate([vp[:, gs], vo[:, gs], vn[:, gs], vc[:, gs]], axis=0)
            q4 = _stack_heads(q[:, gs])
            dy4 = _stack_heads(dy_[:, gs]).astype(BF16)
            lse4 = jnp.max(_stack_heads(lse_[:, gs], fill=-1e30), axis=-1, keepdims=True)
            delta = jnp.sum(_stack_heads(dly[:, gs]), axis=-1, keepdims=True)
            sink = _per_head_rows([sk_ref[4 * g + j] for j in range(4)])
            pr = jnp.where(valid, jnp.exp(_dot(q4, kcat, NT) - lse4), 0.0)
            dsb = (pr * (_dot(dy4, vcat, NT) - delta)).astype(BF16)
            dsink = jnp.exp(sink - lse4) * delta
            for j in range(4):
                dsk_ref[4 * g + j:4 * g + j + 1, :] += jnp.broadcast_to(
                    -jnp.sum(dsink[j * BLK:(j + 1) * BLK], axis=0, keepdims=True), (1, 128))
            dqs.append(_unstack_heads(_dot(dsb, kcat)))
            dkg = _dot(dsb, q4, TN)
            dvg = _dot(pr.astype(BF16), dy4, TN)
            dkw_ref[0, :, gs] = dkg[:3 * BLK]
            dvw_ref[0, :, gs] = dvg[:3 * BLK]
            dkc_ref[:, gs] += dkg[3 * BLK:]
            dvc_ref[:, gs] += dvg[3 * BLK:]
        dq_ref[...] = jnp.concatenate(dqs, axis=1)

    blk, ctx = _attn_specs(nb)
    out = pl.BlockSpec((BLK, ATW), lambda i: (i, 0))
    win = pl.BlockSpec((1, 3 * BLK, ATW), lambda i: (i, 0, 0))
    acc = _full((L, ATW))
    return _pcall(body, name="attn_bwd", grid=(nb,),
                  in_specs=[pl.BlockSpec(memory_space=pltpu.SMEM), blk(0),
                            blk(-1), blk(0), blk(1), ctx, blk(-1), blk(0), blk(1), ctx, out, out, out],
                  out_specs=[out, win, win, acc, acc, _full((8, 128))],
                  out_shape=[jax.ShapeDtypeStruct((s_len, ATW), F32),
                             jax.ShapeDtypeStruct((nb, 3 * BLK, ATW), F32),
                             jax.ShapeDtypeStruct((nb, 3 * BLK, ATW), F32),
                             jax.ShapeDtypeStruct((L, ATW), F32), jax.ShapeDtypeStruct((L, ATW), F32),
                             jax.ShapeDtypeStruct((8, 128), F32)], carry=carry)(
                      sinks, qr, k4, k4, k4, k4, v4, v4, v4, v4, y, lse, dy)


def _attn_post(p, cos, sin, qnw8, knw2, bd512, bd128, dupt, dq, dkw, dvw, dkc, dvc, dp, carry=None):
    tt = p.shape[0]
    s_len = tt - L
    nb = s_len // BLK
    nctx = L // BLK

    def body(q_ref, kv_ref, cos_ref, sin_ref, qw_ref, kw_ref, b5_ref, b1_ref, dupt_ref,
             dq_ref, kwp, kwo, kwn, vwp, vwo, vwn, dkc_ref, dvc_ref, _dp_in,
             dp_ref, dqw_ref, dkw_ref):
        t = pl.program_id(0)
        j = t - nctx

        @pl.when(t == 0)
        def _():
            dqw_ref[...] = jnp.zeros_like(dqw_ref)
            dkw_ref[...] = jnp.zeros_like(dkw_ref)

        is_lat = t >= nctx
        cos_, sin_ = cos_ref[...], sin_ref[...]
        has_p = is_lat & (j >= 1)
        has_n = is_lat & (j <= nb - 2)
        dk4 = (jnp.where(is_lat, kwo[0], dkc_ref[...]) + jnp.where(has_p, kwp[0], 0.0)
               + jnp.where(has_n, kwn[0], 0.0))
        dv4 = (jnp.where(is_lat, vwo[0], dvc_ref[...]) + jnp.where(has_p, vwp[0], 0.0)
               + jnp.where(has_n, vwn[0], 0.0))
        dkr = _dot(dk4, dupt_ref[...], prec=HI)
        dv = _dot(dv4, dupt_ref[...], prec=HI)
        kv = kv_ref[...]
        k = kv[:, :128]
        kw = kw_ref[...]
        rk = lax.rsqrt(_head_mean(k * k, b1_ref[...]) + EPS)
        xk = k * rk
        dkn = dkr * cos_ + _rot(dkr * sin_)
        dxk = dkn * kw
        dk = rk * (dxk - xk * _head_mean(dxk * xk, b1_ref[...]))
        dkw_ref[...] += jnp.sum(dkn * xk, axis=0, keepdims=True)
        q = q_ref[...]
        qw = qw_ref[...]
        rq = lax.rsqrt(_head_mean(q * q, b5_ref[...]) + EPS)
        xq = q * rq
        cos4 = jnp.concatenate([cos_] * 4, axis=1)
        sin4 = jnp.concatenate([sin_] * 4, axis=1)
        dqr = jnp.where(is_lat, dq_ref[...], 0.0) * (HDIM ** -0.5)
        dqn = dqr * cos4 + _rot(dqr * sin4)
        dxq = dqn * qw
        dqraw = rq * (dxq - xq * _head_mean(dxq * xq, b5_ref[...]))
        dqw_ref[...] += jnp.sum(dqn * xq, axis=0, keepdims=True)
        dp_ref[...] = jnp.concatenate([dqraw, dk, dv], axis=1).astype(BF16)

    row = lambda w, cb: pl.BlockSpec((BLK, w), lambda t: (t, cb))
    lat = pl.BlockSpec((BLK, ATW), lambda t: (jnp.maximum(t - nctx, 0), 0))

    def part(off):
        return pl.BlockSpec((1, BLK, ATW), lambda t: (jnp.clip(t - nctx + off, 0, nb - 1), 1 - off, 0))

    cacc = pl.BlockSpec((BLK, ATW), lambda t: (jnp.minimum(t, nctx - 1), 0))
    return _pcall(body, name="attn_post", grid=(tt // BLK,),
                  in_specs=[row(ATW, C_QRAW), row(256, C_KV), row(128, 0), row(128, 0),
                            _full((1, ATW)), _full((1, 128)), _full((ATW, ATW)), _full((128, 128)),
                            _full((ATW, 128)), lat, part(-1), part(0), part(1), part(-1), part(0), part(1),
                            cacc, cacc, ANY],
                  out_specs=[pl.BlockSpec((BLK, 768), lambda t: (t, C_QKV)), _full((1, ATW)), _full((1, 128))],
                  out_shape=[jax.ShapeDtypeStruct(dp.shape, BF16), jax.ShapeDtypeStruct((1, ATW), F32),
                             jax.ShapeDtypeStruct((1, 128), F32)],
                  aliases={18: 0}, carry=carry)(p, p, cos, sin, qnw8, knw2, bd512, bd128, dupt,
                                   dq, dkw, dkw, dkw, dvw, dvw, dvw, dkc, dvc, dp)


def _branch_merge(y_hg, y_at, bh4, ba4, p):
    s_len = y_hg.shape[0]

    def body(yh_ref, ya_ref, bh_ref, ba_ref, gh_ref, ga_ref, ah_ref, aa_ref, m_ref):
        yh, ya = yh_ref[...], ya_ref[...]
        ah = jnp.concatenate([_bdot(yh, bh_ref[j]) for j in range(4)], axis=1)
        aa = jnp.concatenate([_bdot(ya, ba_ref[j]) for j in range(4)], axis=1)
        ah_ref[...] = ah.astype(BF16)
        aa_ref[...] = aa.astype(BF16)
        m_ref[...] = (_sig(gh_ref[...]) * ah + _sig(ga_ref[...]) * aa).astype(BF16)

    row = pl.BlockSpec((TM, D), lambda i: (i, 0))
    y = pl.BlockSpec((TM, HGW), lambda i: (i, 0))
    f = jax.ShapeDtypeStruct((s_len, D), BF16)
    return _pcall(body, name="branch_merge", grid=(s_len // TM,),
                  in_specs=[y, y, _full(bh4.shape), _full(ba4.shape),
                            pl.BlockSpec((TM, D), lambda i: (i + 1, 2)), pl.BlockSpec((TM, D), lambda i: (i + 1, 3))],
                  out_specs=[row, row, row],
                  out_shape=[f, f, jax.ShapeDtypeStruct((s_len, D), BF16)])(y_hg, y_at, bh4, ba4, p, p)


def _branch_bwd(dmh, dma, bh4, ba4, y_hg, y_at):
    s_len = dmh.shape[0]
    nk = s_len // TS
    ns = D // 4

    def body(dh_ref, da_ref, bh_ref, ba_ref, yh_ref, ya_ref, dyh_ref, dya_ref, gh_ref, ga_ref, acc_h, acc_a):
        t = pl.program_id(0)

        @pl.when(t == 0)
        def _():
            acc_h[...] = jnp.zeros_like(acc_h)
            acc_a[...] = jnp.zeros_like(acc_a)

        for d_ref, w_ref, y_ref, dy_ref, acc in ((dh_ref, bh_ref, yh_ref, dyh_ref, acc_h),
                                                 (da_ref, ba_ref, ya_ref, dya_ref, acc_a)):
            y = y_ref[...]
            dy = jnp.zeros((TS, HGW), F32)
            for j in range(4):
                dj = d_ref[:, j * ns:(j + 1) * ns]
                dy = dy + _bdot(dj, w_ref[j], NT)
                acc[j] += _bdot(y, dj, TN)
            dy_ref[...] = dy

        @pl.when(t == nk - 1)
        def _():
            gh_ref[...] = acc_h[...].astype(BF16)
            ga_ref[...] = acc_a[...].astype(BF16)

    dm = pl.BlockSpec((TS, D), lambda t: (t, 0))
    y = pl.BlockSpec((TS, HGW), lambda t: (t, 0))
    w = _full(bh4.shape)
    fy = jax.ShapeDtypeStruct((s_len, HGW), F32)
    gw = jax.ShapeDtypeStruct(bh4.shape, BF16)
    return _pcall(body, name="branch_bwd", grid=(nk,), in_specs=[dm, dm, w, w, y, y],
                  out_specs=[y, y, w, w], out_shape=[fy, fy, gw, gw],
                  scratch=[pltpu.VMEM(bh4.shape, F32)] * 2)(dmh, dma, bh4, ba4, y_hg, y_at)


def _merge_bwd(dattn, w_o, mixed, ah, aa, p, carry=None):
    tt = p.shape[0]
    s_len = tt - L
    nt = tt // TM

    def body(da_ref, wo_ref, mx_ref, ah_ref, aa_ref, gh_ref, ga_ref, dp_ref, dmh_ref, dma_ref, go_ref, acc):
        i = pl.program_id(0)

        @pl.when(i == 0)
        def _():
            dp_ref[...] = jnp.zeros_like(dp_ref)
            acc[...] = jnp.zeros_like(acc)

        @pl.when(i >= 1)
        def _():
            da = da_ref[...]
            acc[...] += _bdot(mx_ref[...], da, TN)
            dm_ = _bdot(da, wo_ref[...], NT)
            sh, sa = _sig(gh_ref[...]), _sig(ga_ref[...])
            dp_ref[...] = jnp.concatenate([dm_ * ah_ref[...].astype(F32) * sh * (1.0 - sh),
                                           dm_ * aa_ref[...].astype(F32) * sa * (1.0 - sa)], axis=1).astype(BF16)
            dmh_ref[...] = (dm_ * sh).astype(BF16)
            dma_ref[...] = (dm_ * sa).astype(BF16)

        @pl.when(i == nt - 1)
        def _():
            go_ref[...] = acc[...].astype(BF16)

    lat = pl.BlockSpec((TM, D), lambda i: (jnp.maximum(i - 1, 0), 0))
    return _pcall(body, name="merge_bwd", grid=(nt,),
                  in_specs=[lat, _full((D, D)), lat, lat, lat, pl.BlockSpec((TM, D), lambda i: (i, 2)),
                            pl.BlockSpec((TM, D), lambda i: (i, 3))],
                  out_specs=[pl.BlockSpec((TM, 2 * D), lambda i: (i, C_GATES)), lat, lat, _full((D, D))],
                  out_shape=[jax.ShapeDtypeStruct((tt, NCOL), BF16), jax.ShapeDtypeStruct((s_len, D), BF16),
                             jax.ShapeDtypeStruct((s_len, D), BF16), jax.ShapeDtypeStruct((D, D), BF16)],
                  scratch=[pltpu.VMEM((D, D), F32)], carry=carry)(dattn, w_o, mixed, ah, aa, p, p)


def _local_step(x, ctx, tgt, mod, modc, nw1, nw2, lg, hw, qnw, knw, sinks,
                w_in, wts, dist=None):
    s_len = x.shape[0]
    tt = s_len + L
    ss1 = jnp.stack([modc, mod[0:2]])
    ss2 = mod[3:5][None]
    g1, g2 = mod[2:3], mod[5:6]
    hw4 = jnp.tile(hw, (1, 4))
    qnw8 = jnp.tile(qnw, (1, 8))
    knw2 = jnp.tile(knw, (1, 2))
    cos, sin = _rope_tables(s_len)
    bd512, bd128 = _blockdiag(ATW, HDIM), _blockdiag(128, HDIM)
    dupm = _dup_matrix()
    dup, dupt = jnp.asarray(dupm, BF16), jnp.asarray(dupm.T, F32)
    tmt = tt

    def four(b):
        return b.reshape(4, 2 * b.shape[1], b.shape[2])

    def halves(g):
        return g.reshape(4, 2, g.shape[1] // 2, g.shape[2])

    h = _mod1(ctx, x, nw1, ss1)
    if dist is None:
        bh4, ba4, w_o, g4, u4, dn4 = wts
        p = _mm_in(h, w_in, tmt)
        o0, st0 = _hgrn_fwd(p, lg, rev=False)
        o1, st1, y_hg = _hgrn_fwd(p, lg, rev=True, readout=(o0, hw4))
    else:
        core, chip = dist
        half = wts[3].shape[1] // 2
        p, first = _mm_in(h, w_in, tmt, carry=_carry_join(_carry_gather(list(wts[0:3])),
                                                          _carry_gather([wts[3]], rows=[(0, half)])))
        (o0, st0), (g8,) = _hgrn_fwd(p, lg, rev=False, carry=_carry_gather([first[3]], rows=[(half, half)]))
        (o1, st1, y_hg), (dn8a,) = _hgrn_fwd(p, lg, rev=True, readout=(o0, hw4),
                                             carry=_carry_gather([wts[5]], rows=[(0, half)]))
        bh4, ba4, w_o, g4 = four(first[0]), four(first[1]), four(first[2]).reshape(D, D), four(g8)
    qr, k4, v4 = _qk_prep(p, cos, sin, qnw8, knw2, bd512, bd128, dup)
    if dist is None:
        y_at, lse = _attn_fwd(qr, k4, v4, sinks)
    else:
        (y_at, lse), (u8, dn8) = _attn_fwd(qr, k4, v4, sinks,
                                           carry=_carry_gather([wts[4], dn8a], rows=[None, (half, half)]))
        u4, dn4 = four(u8), four(dn8)
    ah, aa, mixed = _branch_merge(y_hg, y_at, bh4, ba4, p)
    ao, x1, h2 = _out_proj_mod2(mixed, w_o, x, g1, nw2, ss2)
    a4, b4, z4 = _ffn_up(h2, g4, u4)
    sq, dx2, dyb, dg2 = _ffn_down_loss(z4, dn4, x1, g2, tgt)

    da4, db4 = _ffn_dz(dyb, dn4, a4, b4)
    g_dn = _ffn_gdn(z4, dyb)
    if dist is None:
        dh2 = _ffn_dh2(da4, db4, g4, u4)
    else:
        dn_units = [halves(g_dn)]
        dh2, dn_recv = _ffn_dh2(da4, db4, g4, u4, carry=_carry_pairx(dn_units))
        dn_pairs = _rs_pair_add(dn_units, dn_recv, core)
    if dist is None:
        g_g, g_u = _ffn_ggu(h2, da4, db4)
    else:
        (g_g, g_u), c_dn = _ffn_ggu(h2, da4, db4, carry=_carry_chipx(dn_pairs))
        red_dn = _rs_chip_add(dn_pairs, c_dn, core, chip)
    dx1, dattn, dss2, dnw2, dg1 = _mod2_bwd(x1, dh2, dx2, ao, nw2, ss2, g1)
    if dist is None:
        dp, dmh, dma, g_o = _merge_bwd(dattn, w_o, mixed, ah, aa, p)
    else:
        gu_units = [halves(g_g), halves(g_u)]
        (dp, dmh, dma, g_o), gu_recv = _merge_bwd(dattn, w_o, mixed, ah, aa, p, carry=_carry_pairx(gu_units))
        ffn_pairs = list(dn_pairs) + list(_rs_pair_add(gu_units, gu_recv, core))
    dy_hg, dy_at, g_bh, g_ba = _branch_bwd(dmh, dma, bh4, ba4, y_hg, y_at)
    if dist is None:
        dp, do, dhw4 = _readout_bwd(o0, o1, p, hw4, dy_hg, dp)
        dq, dkw, dvw, dkc, dvc, dsk = _attn_bwd(qr, k4, v4, sinks, y_at, lse, dy_at)
        dp, dqnw8, dknw2 = _attn_post(p, cos, sin, qnw8, knw2, bd512, bd128, dupt, dq, dkw, dvw, dkc, dvc, dp)
    else:
        mix_units = [halves(g_bh), halves(g_ba), halves(g_o.reshape(4, D // 4, D))]
        (dp, do, dhw4), mix_recv = _readout_bwd(o0, o1, p, hw4, dy_hg, dp, carry=_carry_pairx(mix_units))
        mix_pairs = _rs_pair_add(mix_units, mix_recv, core)
        (dq, dkw, dvw, dkc, dvc, dsk), bwd = _attn_bwd(
            qr, k4, v4, sinks, y_at, lse, dy_at,
            carry=_carry_join(_carry_chipx(ffn_pairs[1:2]), _carry_sibx(red_dn)))
        red_g = _rs_chip_add(ffn_pairs[1:2], bwd[0:1], core, chip)
        (dp, dqnw8, dknw2), post = _attn_post(
            p, cos, sin, qnw8, knw2, bd512, bd128, dupt, dq, dkw, dvw, dkc, dvc, dp, carry=_carry_sibx(red_g))
    if dist is None:
        dp, dv0, dq0, dlg0 = _hgrn_bwd(p, lg, do, st0, dp, None, rev=False)
        dp, dlg1 = _hgrn_bwd(p, lg, do, st1, dp, (dv0, dq0), rev=True)
    else:
        (dp, dv0, dq0, dlg0), c_u = _hgrn_bwd(p, lg, do, st0, dp, None, rev=False,
                                              carry=_carry_chipx(ffn_pairs[2:3]))
        red_u = _rs_chip_add(ffn_pairs[2:3], c_u, core, chip)
        (dp, dlg1), last = _hgrn_bwd(p, lg, do, st1, dp, (dv0, dq0), rev=True,
                                     carry=_carry_join(_carry_chipx(mix_pairs), _carry_sibx(red_u)))
        mix_reds = _rs_chip_add(mix_pairs, last[0:3], core, chip)
        ffn_done = bwd[1:2] + post[0:1] + last[3:4]
    g_in = _mm_gin(dp, h, tmt)
    if dist is None:
        dh = _mm_dh(dp, w_in, tmt)
        gx, dss1, dnw1 = _mod1_bwd(ctx, x, dh, dx1, nw1, ss1)
        rs = None
    else:
        in_units = [halves(g_in.reshape(4, NCOL // 4, D))]
        in_pairs = _rs_pair_add(in_units, _pair_exchange(in_units), core)
        first_rows, rest_rows = _split_rows(in_pairs[0].shape[1], IN_ROWS_WITH_DH)
        dh, both = _mm_dh(dp, w_in, tmt, carry=_carry_join(_carry_chipx(in_pairs, rows=first_rows),
                                                           _carry_sibx(mix_reds)))
        in_part, mix_done = both[0:1], both[1:4]
        gx, dss1, dnw1 = _mod1_bwd(ctx, x, dh, dx1, nw1, ss1)
        rs = dict(ffn_done=ffn_done, mix_done=mix_done, in_pairs=in_pairs, in_part=in_part, rest_rows=rest_rows)

    dmod = jnp.concatenate([dss1[1], dg1, dss2, dg2], axis=0)
    dmodc = dss1[0]
    raw = (dss1, dg1, dss2, dg2, dnw1, dnw2, dhw4, dqnw8, dknw2, dsk, dlg0, dlg1)
    small = dict(raw=raw, dmod=dmod, dmodc=dmodc, dnw1=dnw1, dnw2=dnw2,
                 dhw=dhw4.reshape(4, HGD).sum(0, keepdims=True),
                 dqnw=dqnw8.reshape(8, HDIM).sum(0, keepdims=True),
                 dknw=dknw2.reshape(2, HDIM).sum(0, keepdims=True),
                 dsinks=dsk[:, 0], dlg=jnp.concatenate([dlg0, dlg1], axis=0))
    big = dict(w_in=g_in, w_bh=g_bh, w_ba=g_ba, w_o=g_o, w_g=g_g, w_u=g_u, w_dn=g_dn)
    return sq, gx, big, small, rs


def _place():
    x, y, c = lax.axis_index("x"), lax.axis_index("y"), lax.axis_index("c")
    return x, y, c


def _gather_blocks(x_refs, out_refs, send_sems, recv_sems, local_sems):
    n = len(out_refs)
    x, y, c = _place()
    me, sibling = (x, y, c), (x, y, 1 - c)
    chips = [(1 - x, y), (x, 1 - y), (1 - x, 1 - y)]

    def slot(u, px, py, pc):
        return out_refs[u].at[4 * px + 2 * py + pc]

    def copy(u, k, block, to, src=None):
        return pltpu.make_async_remote_copy(
            src_ref=slot(u, *block) if src is None else src, dst_ref=slot(u, *block),
            send_sem=send_sems.at[u, k], recv_sem=recv_sems.at[u, k], device_id=to, device_id_type=MESH)

    mines = [pltpu.make_async_copy(x_refs[u], slot(u, *me), local_sems.at[u]) for u in range(n)]
    for cp in mines:
        cp.start()
    first = []
    for u in range(n):
        first.append(copy(u, 0, me, sibling, src=x_refs[u]))
        first += [copy(u, 1 + j, me, (*chip, c), src=x_refs[u]) for j, chip in enumerate(chips)]
    for cp in first:
        cp.start()
    passed = []
    for j, chip in enumerate(chips):
        for u in range(n):
            copy(u, 1 + j, (*chip, c), me).wait_recv()
            fwd = copy(u, 4 + j, (*chip, c), sibling)
            fwd.start()
            passed.append(fwd)
    for u in range(n):
        copy(u, 0, sibling, me).wait_recv()
    for j, chip in enumerate(chips):
        for u in range(n):
            copy(u, 4 + j, (*chip, 1 - c), me).wait_recv()
    for cp in first + passed:
        cp.wait_send()
    for cp in mines:
        cp.wait()


def _gather_sems(n):
    return [pltpu.SemaphoreType.DMA((n, 7)), pltpu.SemaphoreType.DMA((n, 7)), pltpu.SemaphoreType.DMA((n,))]


def _allgather(blks, *, name):
    n = len(blks)
    vm = pl.BlockSpec(memory_space=pltpu.VMEM)

    def body(*refs):
        _peer_barrier("both")
        _gather_blocks(refs[:n], refs[n:2 * n], *refs[2 * n:])

    return pl.pallas_call(
        body, name=name, out_shape=[jax.ShapeDtypeStruct((8,) + b.shape, b.dtype) for b in blks],
        in_specs=[vm] * n, out_specs=[vm] * n, scratch_shapes=_gather_sems(n),
        compiler_params=pltpu.CompilerParams(collective_id=BARRIER_IDS["both"]))(*blks)


def _cast_place(ws, c, dev):
    n = len(ws)

    def body(s_ref, *refs):
        for u in range(n):
            refs[n + u][0] = refs[u][...].astype(BF16)

    in_specs, out_specs, out_shape = [], [], []
    for w in ws:
        q, cols = w.shape[0] // 4, w.shape[1]
        in_specs.append(pl.BlockSpec((q, cols), lambda i, s: (2 * s[0] + i, 0)))
        out_specs.append(pl.BlockSpec((1, q, cols), lambda i, s: (s[1], i, 0)))
        out_shape.append(jax.ShapeDtypeStruct((8, 2 * q, cols), BF16))
    return pl.pallas_call(
        body, name="cast_place",
        grid_spec=pltpu.PrefetchScalarGridSpec(num_scalar_prefetch=1, grid=(2,), in_specs=in_specs,
                                               out_specs=out_specs),
        out_shape=_out_hbm(out_shape),
        compiler_params=pltpu.CompilerParams(vmem_limit_bytes=48 << 20))(jnp.stack([c, dev]), *_in_hbm(ws))


def _gather_phases(out_refs, send_sems, recv_sems, rows=None):
    n = len(out_refs)
    x, y, c = _place()
    me, sibling = (x, y, c), (x, y, 1 - c)
    chips = [(1 - x, y), (x, 1 - y), (1 - x, 1 - y)]

    def copy(u, k, block, to):
        px, py, pc = block
        ref = out_refs[u].at[4 * px + 2 * py + pc]
        if rows is not None and rows[u] is not None:
            ref = ref.at[pl.ds(rows[u][0], rows[u][1])]
        return pltpu.make_async_remote_copy(src_ref=ref, dst_ref=ref, send_sem=send_sems.at[u, k],
                                            recv_sem=recv_sems.at[u, k], device_id=to, device_id_type=MESH)

    def start():
        for u in range(n):
            copy(u, 0, me, sibling).start()
            for j, chip in enumerate(chips):
                copy(u, 1 + j, me, (*chip, c)).start()

    def mid():
        for j, chip in enumerate(chips):
            for u in range(n):
                copy(u, 1 + j, (*chip, c), me).wait_recv()
                copy(u, 4 + j, (*chip, c), sibling).start()

    def end():
        for u in range(n):
            copy(u, 0, sibling, me).wait_recv()
        for j, chip in enumerate(chips):
            for u in range(n):
                copy(u, 4 + j, (*chip, 1 - c), me).wait_recv()
        for u in range(n):
            copy(u, 0, me, sibling).wait_send()
            for j, chip in enumerate(chips):
                copy(u, 1 + j, me, (*chip, c)).wait_send()
                copy(u, 4 + j, (*chip, c), sibling).wait_send()

    return start, mid, end


def _carry_gather(bufs, rows=None):
    n = len(bufs)
    return _Carry(bufs, [jax.ShapeDtypeStruct(b.shape, b.dtype) for b in bufs], {u: u for u in range(n)},
                  [pltpu.SemaphoreType.DMA((n, 7)), pltpu.SemaphoreType.DMA((n, 7))],
                  lambda ins, outs, sems: _gather_phases(outs, *sems, rows=rows), "both")


def _ag_small(raw, sq):
    def body(dss1, dg1, dss2, dg2, dnw1, dnw2, dhw4, dqnw8, dknw2, dsk, dlg0, dlg1, sq_ref,
             out_ref, tot_ref, blk, send_sems, recv_sems, local_sems):
        _peer_barrier("both")
        blk[...] = jnp.zeros_like(blk)
        blk[0:2, :] = dss1[1]
        blk[2:3, :] = dg1[...]
        blk[3:5, :] = dss2[...]
        blk[5:6, :] = dg2[...]
        blk[6:8, :] = dss1[0]
        blk[8:9, :] = dnw1[...]
        blk[9:10, :] = dnw2[...]
        blk[10:11, 0:HGW] = dhw4[...]
        blk[10:11, HGW:D] = dqnw8[...]
        blk[11:12, 0:128] = dknw2[...]
        blk[12:14, 0:HGW] = dlg0[0]
        blk[14:16, 0:HGW] = dlg1[0]
        blk[16:24, 0:128] = dsk[...]
        blk[24:25, :] = sq_ref[...]
        _gather_blocks([blk], [out_ref], send_sems, recv_sems, local_sems)
        acc = out_ref[0]
        for i in range(1, 8):
            acc = acc + out_ref[i]
        tot_ref[...] = acc

    vm = pl.BlockSpec(memory_space=pltpu.VMEM)
    return pl.pallas_call(
        body, name="ag_small",
        out_shape=[jax.ShapeDtypeStruct((8, 32, D), F32), jax.ShapeDtypeStruct((32, D), F32)],
        in_specs=[vm] * 13, out_specs=[vm, vm],
        scratch_shapes=[pltpu.VMEM((32, D), F32)] + _gather_sems(1),
        compiler_params=pltpu.CompilerParams(collective_id=BARRIER_IDS["both"]))(*raw, sq)


def _pairx_shapes(units):
    return [jax.ShapeDtypeStruct((4,) + g.shape[2:], g.dtype) for g in units]


def _pairx_phases(g_refs, r_refs, send_sems, recv_sems):
    n = len(g_refs)
    x, y, c = _place()
    cps = [pltpu.make_async_remote_copy(
        src_ref=g_refs[u].at[j, 1 - c], dst_ref=r_refs[u].at[j], send_sem=send_sems.at[u, j],
        recv_sem=recv_sems.at[u, j], device_id=(x, y, 1 - c), device_id_type=MESH)
        for u in range(n) for j in range(4)]

    def start():
        for cp in cps:
            cp.start()

    def end():
        for cp in cps:
            cp.wait()

    return start, None, end


def _carry_pairx(units):
    n = len(units)
    return _Carry(units, _pairx_shapes(units), {},
                  [pltpu.SemaphoreType.DMA((n, 4)), pltpu.SemaphoreType.DMA((n, 4))],
                  lambda ins, outs, sems: _pairx_phases(ins, outs, *sems), "sib")


def _pair_exchange(units):
    n = len(units)

    def body(*refs):
        _peer_barrier("sib")
        start, _, end = _pairx_phases(refs[:n], refs[n:2 * n], *refs[2 * n:])
        start()
        end()

    return pl.pallas_call(
        body, name="pair_exchange", out_shape=_pairx_shapes(units), in_specs=[ANY] * n, out_specs=[ANY] * n,
        scratch_shapes=[pltpu.SemaphoreType.DMA((n, 4))] * 2,
        compiler_params=pltpu.CompilerParams(collective_id=BARRIER_IDS["sib"]))(*units)


IN_ROWS_WITH_DH = 0.6


def _split_rows(h, share):
    first = int(h * share) // BF16_SUBLANES * BF16_SUBLANES
    return (0, first), (first, h - first)


def _rs_pair_add(units, recvs, c):
    n = len(units)

    def body(c_ref, *refs):
        for u in range(n):
            refs[2 * n + u][...] = (refs[u][0].astype(F32) + refs[n + u][...].astype(F32)).astype(BF16)

    in_specs, out_specs, out_shape = [], [], []
    for g in units:
        h, w = g.shape[2:]
        in_specs.append(pl.BlockSpec((1, 1, h, w), lambda j, cr: (j, cr[0], 0, 0)))
    for g in units:
        h, w = g.shape[2:]
        in_specs.append(pl.BlockSpec((1, h, w), lambda j, cr: (j, 0, 0)))
        out_specs.append(pl.BlockSpec((1, h, w), lambda j, cr: (j, 0, 0)))
        out_shape.append(jax.ShapeDtypeStruct((4, h, w), BF16))
    return pl.pallas_call(
        body, name="rs_pair_add",
        grid_spec=pltpu.PrefetchScalarGridSpec(num_scalar_prefetch=1, grid=(4,), in_specs=in_specs,
                                               out_specs=out_specs),
        out_shape=_out_hbm(out_shape),
        compiler_params=pltpu.CompilerParams(vmem_limit_bytes=48 << 20))(
            c.reshape(1), *_in_hbm(list(units) + list(recvs)))


def _chipx_phases(p_refs, r_refs, send_sems, recv_sems, rows=None):
    n = len(p_refs)
    x, y, c = _place()
    k = 2 * x + y

    def part(ref):
        return ref if rows is None else ref.at[pl.ds(rows[0], rows[1])]

    sends = []
    for d in range(1, 4):
        j = (k + d) % 4
        for u in range(n):
            sends.append(pltpu.make_async_remote_copy(
                src_ref=part(p_refs[u].at[j]), dst_ref=part(r_refs[u].at[k]), send_sem=send_sems.at[u, d - 1],
                recv_sem=recv_sems.at[u, d - 1], device_id=(j // 2, j % 2, c), device_id_type=MESH))

    def start():
        for cp in sends:
            cp.start()

    def end():
        for d in range(1, 4):
            src = (k + 4 - d) % 4
            for u in range(n):
                pltpu.make_async_remote_copy(
                    src_ref=part(p_refs[u].at[src]), dst_ref=part(r_refs[u].at[src]),
                    send_sem=send_sems.at[u, d - 1], recv_sem=recv_sems.at[u, d - 1], device_id=(x, y, c),
                    device_id_type=MESH).wait_recv()
        for cp in sends:
            cp.wait_send()

    return start, None, end


def _carry_chipx(pairs, rows=None, into=None):
    n = len(pairs)
    sems = [pltpu.SemaphoreType.DMA((n, 3)), pltpu.SemaphoreType.DMA((n, 3))]
    shapes = [jax.ShapeDtypeStruct(p.shape, p.dtype) for p in pairs]
    if into is None:
        return _Carry(pairs, shapes, {}, sems, lambda ins, outs, s: _chipx_phases(ins, outs, *s, rows=rows),
                      "chips")
    return _Carry(list(pairs) + list(into), shapes, {n + u: u for u in range(n)}, sems,
                  lambda ins, outs, s: _chipx_phases(ins[:n], outs, *s, rows=rows), "chips")


def _rs_chip_add(pairs, contribs, c, chip):
    n = len(pairs)

    def body(s_ref, *refs):
        for u in range(n):
            a, b, c_, d = refs[4 * u:4 * u + 4]
            refs[4 * n + u][0] = ((a[0].astype(F32) + b[0].astype(F32)) + c_[0].astype(F32)) + d[0].astype(F32)

    in_specs, out_specs, out_shape, args = [], [], [], []
    for p, r in zip(pairs, contribs):
        h, w = p.shape[1] // 2, p.shape[2]
        in_specs += [pl.BlockSpec((1, h, w), functools.partial(lambda d, i, s: ((s[1] + d) % 4, i, 0), d))
                     for d in range(4)]
        args += [p, r, r, r]
        out_specs.append(pl.BlockSpec((1, h, w), lambda i, s: (s[0], i, 0)))
        out_shape.append(jax.ShapeDtypeStruct((2, 2 * h, w), F32))
    return pl.pallas_call(
        body, name="rs_chip_add",
        grid_spec=pltpu.PrefetchScalarGridSpec(num_scalar_prefetch=1, grid=(2,), in_specs=in_specs,
                                               out_specs=out_specs),
        out_shape=_out_hbm(out_shape),
        compiler_params=pltpu.CompilerParams(vmem_limit_bytes=48 << 20))(jnp.stack([c, chip]), *_in_hbm(args))


def _rs_sibling_gather(reds):
    n = len(reds)

    def body(*refs):
        _peer_barrier("sib")
        start, _, end = _sibx_phases(refs[n:2 * n], *refs[2 * n:])
        start()
        end()

    return pl.pallas_call(
        body, name="rs_sibling_gather", out_shape=[jax.ShapeDtypeStruct(r.shape, r.dtype) for r in reds],
        in_specs=[ANY] * n, out_specs=[ANY] * n, input_output_aliases={u: u for u in range(n)},
        scratch_shapes=[pltpu.SemaphoreType.DMA((n,))] * 2,
        compiler_params=pltpu.CompilerParams(collective_id=BARRIER_IDS["sib"]))(*reds)


def _sibx_phases(o_refs, send_sems, recv_sems):
    n = len(o_refs)
    x, y, c = _place()
    cps = [pltpu.make_async_remote_copy(
        src_ref=o_refs[u].at[c], dst_ref=o_refs[u].at[c], send_sem=send_sems.at[u], recv_sem=recv_sems.at[u],
        device_id=(x, y, 1 - c), device_id_type=MESH) for u in range(n)]

    def start():
        for cp in cps:
            cp.start()

    def end():
        for u in range(n):
            cps[u].wait_send()
            pltpu.make_async_remote_copy(
                src_ref=o_refs[u].at[1 - c], dst_ref=o_refs[u].at[1 - c], send_sem=send_sems.at[u],
                recv_sem=recv_sems.at[u], device_id=(x, y, 1 - c), device_id_type=MESH).wait_recv()

    return start, None, end


def _carry_sibx(reds):
    n = len(reds)
    return _Carry(reds, [jax.ShapeDtypeStruct(r.shape, r.dtype) for r in reds], {u: u for u in range(n)},
                  [pltpu.SemaphoreType.DMA((n,))] * 2, lambda ins, outs, sems: _sibx_phases(outs, *sems), "sib")


def _prologue(blk, c_ctx, w, b, in8):
    n = w.shape[1]

    def body(blk_ref, cctx_ref, w_ref, b_ref, _in_in, g0_ref, c16_ref, g1_ref, in_ref, s1, r1, l1, s2, r2, s3, r3):
        _peer_barrier("both")
        x, y, c = _place()
        start, mid, end = _gather_phases([in_ref], s3, r3)
        start_mod, mid_mod, end_mod = _gather_phases([g1_ref], s2, r2)
        _gather_blocks([blk_ref], [g0_ref], s1, r1, l1)
        start()
        c16 = jnp.concatenate([g0_ref[i, 0:1, :] for i in range(8)] + [cctx_ref[...], jnp.zeros((7, D), F32)],
                              axis=0)
        c16_ref[...] = c16
        g1_ref[4 * x + 2 * y + c] = _dot(c16 * _sig(c16), w_ref[...], prec=HI) + b_ref[...]
        start_mod()
        mid()
        mid_mod()
        end()
        end_mod()

    vm = pl.BlockSpec(memory_space=pltpu.VMEM)
    return pl.pallas_call(
        body, name="prologue",
        out_shape=[jax.ShapeDtypeStruct((8, 8, D), F32), jax.ShapeDtypeStruct((16, D), F32),
                   jax.ShapeDtypeStruct((8, 16, n), F32), jax.ShapeDtypeStruct(in8.shape, in8.dtype)],
        in_specs=[vm, vm, vm, vm, ANY], out_specs=[vm, vm, vm, ANY], input_output_aliases={4: 3},
        scratch_shapes=_gather_sems(1) + [pltpu.SemaphoreType.DMA((1, 7))] * 4,
        compiler_params=pltpu.CompilerParams(vmem_limit_bytes=48 << 20,
                                             collective_id=BARRIER_IDS["both"]))(blk, c_ctx, w, b, in8)


def _ada_bwd(c16, dmod16, w, carry=None):
    n = w.shape[1]
    tn = 512

    def body(c_ref, d_ref, w_ref, gw_ref, gc_ref):
        j = pl.program_id(0)

        @pl.when(j == 0)
        def _():
            gc_ref[...] = jnp.zeros_like(gc_ref)

        cc = c_ref[...]
        dm = d_ref[...]
        gw_ref[...] = _dot(cc * _sig(cc), dm, TN, prec=HI)
        gc_ref[...] += _dot(dm, w_ref[...], NT, prec=HI)

    return _pcall(body, name="ada_bwd", grid=(n // tn,),
                  in_specs=[_full((16, D)), pl.BlockSpec((16, tn), lambda j: (0, j)),
                            pl.BlockSpec((D, tn), lambda j: (0, j))],
                  out_specs=[pl.BlockSpec((D, tn), lambda j: (0, j)), _full((16, D))],
                  out_shape=[jax.ShapeDtypeStruct((D, n), F32),
                             jax.ShapeDtypeStruct((16, D), F32)], carry=carry)(c16, dmod16, w)


def _adam_math(w, g, m, v):
    c1 = 1.0 - ADAM_B1 ** ADAM_STEP
    c2 = 1.0 - ADAM_B2 ** ADAM_STEP
    nm = ADAM_B1 * m + (1.0 - ADAM_B1) * g
    nv = ADAM_B2 * v + (1.0 - ADAM_B2) * (g * g)
    return -ADAM_LR * ((nm / c1) / (jnp.sqrt(nv / c2) + ADAM_EPS) + ADAM_WD * w), nm, nv


def _adamw_small(ws, gs, ms, vs):
    n = len(ws)

    def body(*refs):
        for u in range(n):
            d_, nm, nv = _adam_math(refs[u][...], refs[n + u][...], refs[2 * n + u][...], refs[3 * n + u][...])
            refs[4 * n + u][...] = d_
            refs[5 * n + u][...] = nm
            refs[6 * n + u][...] = nv

    specs = [_full(w.shape) for w in ws]
    shapes = [jax.ShapeDtypeStruct(w.shape, F32) for w in ws]
    out = _pcall(body, name="adamw_small", grid=(1,), in_specs=specs * 4, out_specs=specs * 3,
                 out_shape=shapes * 3)(*ws, *gs, *ms, *vs)
    return out[:n], out[n:2 * n], out[2 * n:]


def _cctx_grad(parts, c_ctx):
    def body(p_ref, c_ref, o_ref):
        acc = p_ref[0:1, :]
        for k in range(1, 4):
            acc = acc + p_ref[k:k + 1, :]
        cc = c_ref[...]
        s = _sig(cc)
        o_ref[...] = acc * (s * (1.0 + cc * (1.0 - s)))

    return _pcall(body, name="cctx_grad", grid=(1,), in_specs=[_full(parts.shape), _full((1, D))],
                  out_specs=_full((1, D)), out_shape=jax.ShapeDtypeStruct((1, D), F32))(parts, c_ctx)


ADAM_STEPS = 8


def _adamw_multi(ws, gs, ms, vs, *, name):
    n = len(ws)

    def body(*refs):
        for u in range(n):
            g = refs[n + u][...]
            refs[4 * n + u][...] = g
            refs[5 * n + u][...], refs[6 * n + u][...], refs[7 * n + u][...] = _adam_math(
                refs[u][...], g, refs[2 * n + u][...], refs[3 * n + u][...])

    specs = [pl.BlockSpec((w.shape[0] // ADAM_STEPS, w.shape[1]), lambda i: (i, 0)) for w in ws]
    shapes = [jax.ShapeDtypeStruct(w.shape, F32) for w in ws]
    out = _pcall(body, name=name, grid=(ADAM_STEPS,), in_specs=specs * 4, out_specs=specs * 4,
                 out_shape=shapes * 4)(*ws, *gs, *ms, *vs)
    return out[:n], out[n:2 * n], out[2 * n:3 * n], out[3 * n:]


def kernel(x, c, ctx, c_ctx, w_ada, b_ada, norm_mix_w, norm_ffn_w, w_in, hgrn_lb_logits, hgrn_norm_w, q_norm_w, k_norm_w, attn_sinks, w_branch_hgrn, w_branch_attn, w_out, w_ffn_gate, w_ffn_up, w_ffn_down, loss_target, m_c_ctx, m_w_ada, m_b_ada, m_norm_mix_w, m_norm_ffn_w, m_w_in, m_hgrn_lb_logits, m_hgrn_norm_w, m_q_norm_w, m_k_norm_w, m_attn_sinks, m_w_branch_hgrn, m_w_branch_attn, m_w_out, m_w_ffn_gate, m_w_ffn_up, m_w_ffn_down, v_c_ctx, v_w_ada, v_b_ada, v_norm_mix_w, v_norm_ffn_w, v_w_in, v_hgrn_lb_logits, v_hgrn_norm_w, v_q_norm_w, v_k_norm_w, v_attn_sinks, v_w_branch_hgrn, v_w_branch_attn, v_w_out, v_w_ffn_gate, v_w_ffn_up, v_w_ffn_down):
    xi, yi, ci = _place()
    chip = 2 * xi + yi
    dev = 2 * chip + ci
    s_len = x.shape[1]

    shards = [w_in[0].T, w_branch_hgrn[0], w_branch_attn[0], w_out[0], w_ffn_gate[0].T, w_ffn_up[0].T,
              w_ffn_down[0]]
    bufs = _cast_place(shards, ci, dev)

    lbrow = jnp.pad(hgrn_lb_logits.reshape(1, 512), ((0, 0), (0, D - 512)))
    blk = jnp.concatenate([c, lbrow, jnp.zeros((6, D), F32)], axis=0)
    nada = w_ada.shape[2]
    b_sh = lax.dynamic_slice(b_ada, (0, chip * nada), (1, nada))
    g0, c16, g1, in8 = _prologue(blk, c_ctx[None], w_ada[0], b_sh, bufs[0])
    lg = g0[0::2, 1, :512].reshape(4, 2, 2, 128).transpose(1, 2, 0, 3).reshape(2, 2, HGW)
    modall = g1[0::2].transpose(1, 0, 2).reshape(16, 4 * nada)
    mod = lax.dynamic_slice(modall, (dev, 0), (1, 6 * D)).reshape(6, D)
    modc = modall[8].reshape(6, D)[:2]

    sq, gx, _, small, rs = _local_step(
        x[0], ctx[0], loss_target[0], mod, modc, norm_mix_w, norm_ffn_w, lg, hgrn_norm_w, q_norm_w,
        k_norm_w, attn_sinks[0], in8.reshape(NCOL, D), bufs[1:], dist=(ci, chip))

    def whole(r):
        return r.reshape(2 * r.shape[1], r.shape[2])

    g_dn, g_g, g_u = [whole(r) for r in rs["ffn_done"]]
    g_bh, g_ba, g_o = [whole(r) for r in rs["mix_done"]]
    in_pairs = rs["in_pairs"]

    g2, tot = _ag_small(small["raw"], sq)
    loss = 0.5 * jnp.sum(tot[24]) / D
    dmodc_tot = jnp.pad(tot[6:8].reshape(1, 2 * D), ((0, 0), (0, 4 * D)))
    g_b_ada = tot[0:6].reshape(1, 6 * D) + dmodc_tot
    dmod16 = jnp.concatenate([g2[:, 0:6].reshape(8, 6 * D), dmodc_tot, jnp.zeros((7, 6 * D), F32)], axis=0)
    (g_w_ada, gc_part), in_contribs = _ada_bwd(
        c16, lax.dynamic_slice(dmod16, (0, chip * nada), (16, nada)), w_ada[0],
        carry=_carry_chipx(in_pairs, rows=rs["rest_rows"], into=rs["in_part"]))
    g3, = _allgather([gc_part[8:16]], name="ag_cctx")
    g_c_ctx = _cctx_grad(g3[0::2, 0], c_ctx[None])[0]
    g_nw1 = tot[8:9]
    g_nw2 = tot[9:10]
    g_hw = tot[10, :HGW].reshape(4, HGD).sum(0, keepdims=True)
    g_qnw = tot[10, HGW:].reshape(8, HDIM).sum(0, keepdims=True)
    g_knw = tot[11, :128].reshape(2, HDIM).sum(0, keepdims=True)
    g_sinks = tot[16:24, 0][None]
    g_lg = lax.dynamic_slice(tot[12:16, :HGW].reshape(2, 2, HGW), (0, 0, chip * 128), (2, 2, 128))

    names = ["c_ctx", "w_ada", "b_ada", "norm_mix_w", "norm_ffn_w", "w_in", "hgrn_lb_logits", "hgrn_norm_w",
             "q_norm_w", "k_norm_w", "attn_sinks", "w_branch_hgrn", "w_branch_attn", "w_out", "w_ffn_gate",
             "w_ffn_up", "w_ffn_down"]
    ws = dict(zip(names, [c_ctx, w_ada, b_ada, norm_mix_w, norm_ffn_w, w_in, hgrn_lb_logits, hgrn_norm_w,
                          q_norm_w, k_norm_w, attn_sinks, w_branch_hgrn, w_branch_attn, w_out, w_ffn_gate,
                          w_ffn_up, w_ffn_down]))
    ms = dict(zip(names, [m_c_ctx, m_w_ada, m_b_ada, m_norm_mix_w, m_norm_ffn_w, m_w_in, m_hgrn_lb_logits,
                          m_hgrn_norm_w, m_q_norm_w, m_k_norm_w, m_attn_sinks, m_w_branch_hgrn,
                          m_w_branch_attn, m_w_out, m_w_ffn_gate, m_w_ffn_up, m_w_ffn_down]))
    vs = dict(zip(names, [v_c_ctx, v_w_ada, v_b_ada, v_norm_mix_w, v_norm_ffn_w, v_w_in, v_hgrn_lb_logits,
                          v_hgrn_norm_w, v_q_norm_w, v_k_norm_w, v_attn_sinks, v_w_branch_hgrn,
                          v_w_branch_attn, v_w_out, v_w_ffn_gate, v_w_ffn_up, v_w_ffn_down]))
    transposed = ("w_in", "w_ffn_gate", "w_ffn_up")

    def view(a, n):
        return a[0].T if n in transposed else a[0]

    def unview(a, n):
        return a.T[None] if n in transposed else a[None]

    delta, new_m, new_v, grads = {}, {}, {}, {}

    def big_adamw(group, gs, name):
        g_, d_, m_, v_ = _adamw_multi([view(ws[n], n) for n in group], gs, [view(ms[n], n) for n in group],
                                      [view(vs[n], n) for n in group], name=name)
        for i, n in enumerate(group):
            grads[n], delta[n], new_m[n], new_v[n] = (unview(g_[i], n), unview(d_[i], n), unview(m_[i], n),
                                                      unview(v_[i], n))

    big_adamw(["w_ffn_down", "w_ffn_gate", "w_ffn_up", "w_out", "w_branch_hgrn", "w_branch_attn"],
              [g_dn, g_g, g_u, g_o, g_bh, g_ba], "adamw_first")
    in_reds = _rs_chip_add(in_pairs, in_contribs, ci, chip)
    g_in, = [whole(r) for r in _rs_sibling_gather(in_reds)]
    big_adamw(["w_in", "w_ada"], [g_in, g_w_ada], "adamw_second")
    grads.update(c_ctx=g_c_ctx, b_ada=g_b_ada, norm_mix_w=g_nw1, norm_ffn_w=g_nw2, hgrn_lb_logits=g_lg,
                 hgrn_norm_w=g_hw, q_norm_w=g_qnw, k_norm_w=g_knw, attn_sinks=g_sinks)
    small_names = [n for n in names if n not in delta]

    def two_d(a):
        return a.reshape(1, -1) if a.ndim == 1 else a

    sd, sm_, sv = _adamw_small(*[[two_d(d[n]) for n in small_names] for d in (ws, grads, ms, vs)])
    for i, n in enumerate(small_names):
        for dst, src in ((delta, sd), (new_m, sm_), (new_v, sv)):
            dst[n] = src[i].reshape(ws[n].shape)
    return (loss, gx[None], *[grads[n] for n in names], *[delta[n] for n in names],
            *[new_m[n] for n in names], *[new_v[n] for n in names])
```

```python
import functools

import numpy as np
import jax
import jax.numpy as jnp
from jax import lax
from jax.experimental import pallas as pl
from jax.experimental.pallas import tpu as pltpu

F32 = jnp.float32
BF16 = jnp.bfloat16
HI = lax.Precision.HIGHEST
MESH = pl.DeviceIdType.MESH

D = 1024
L = 256
TM = 256
HGW = 512
HGD = 128
CH = 32
ATW = 512
HDIM = 64
BLK = 128
GRID_W = 64
DFF = 2816
NCOL = 5376
EPS = 1e-6
ROPE_THETA = 10000.0
BF16_SUBLANES = 16

C_FB, C_INP, C_QHG, C_FF = 0, 1, 2, 3
C_GATES = 1
C_GHG, C_QRAW = 8, 9
C_KV = 20
C_QKV = 6

ADAM_LR, ADAM_B1, ADAM_B2, ADAM_EPS, ADAM_WD, ADAM_STEP = 0.001, 0.9, 0.999, 1e-08, 0.01, 10

NN = (((1,), (0,)), ((), ()))
NT = (((1,), (1,)), ((), ()))
TN = (((0,), (0,)), ((), ()))


def _dot(a, b, dims=NN, prec=None):
    return lax.dot_general(a, b, dims, precision=prec, preferred_element_type=F32)


def _bdot(a, b, dims=NN):
    return _dot(a.astype(BF16), b.astype(BF16), dims)


def _sig(x):
    return 1.0 / (1.0 + jnp.exp(-x))


class _Carry:
    def __init__(self, ins, outs, aliases, scratch, phases, peers):
        self.ins, self.outs, self.aliases, self.scratch, self.phases = ins, outs, aliases, scratch, phases
        self.peers = peers


BARRIER_IDS = {"sib": 1, "chips": 2, "both": 3}


def _peer_barrier(kind):
    x, y, c = _place()
    peers = []
    if kind in ("sib", "both"):
        peers.append((x, y, 1 - c))
    if kind in ("chips", "both"):
        peers += [(1 - x, y, c), (x, 1 - y, c), (1 - x, 1 - y, c)]
    bar = pltpu.get_barrier_semaphore()
    for peer in peers:
        pl.semaphore_signal(bar, inc=1, device_id=peer, device_id_type=MESH)
    pl.semaphore_wait(bar, len(peers))


def _in_hbm(args):
    return [pltpu.with_memory_space_constraint(a, pltpu.HBM) for a in args]


def _out_hbm(shapes):
    if isinstance(shapes, (list, tuple)):
        return [pltpu.HBM(s.shape, s.dtype) for s in shapes]
    return pltpu.HBM(shapes.shape, shapes.dtype)


def _carry_join(a, b):
    na_in, na_out, na_sc = len(a.ins), len(a.outs), len(a.scratch)
    aliases = dict(a.aliases)
    aliases.update({na_in + i: na_out + o for i, o in b.aliases.items()})

    def phases(ins, outs, sems):
        pa = a.phases(ins[:na_in], outs[:na_out], sems[:na_sc])
        pb = b.phases(ins[na_in:], outs[na_out:], sems[na_sc:])

        def both(fa, fb):
            if fa is None and fb is None:
                return None

            def run():
                for fn in (fa, fb):
                    if fn is not None:
                        fn()
            return run

        return tuple(both(fa, fb) for fa, fb in zip(pa, pb))

    return _Carry(list(a.ins) + list(b.ins), list(a.outs) + list(b.outs), aliases,
                  list(a.scratch) + list(b.scratch), phases, a.peers if a.peers == b.peers else "both")


def _pcall(body, *, name, grid, in_specs, out_specs, out_shape, scratch=(), aliases=None, vmem_mb=48,
           carry=None):
    params = pltpu.CompilerParams(dimension_semantics=("arbitrary",) * len(grid),
                                  vmem_limit_bytes=vmem_mb << 20)
    if carry is None:
        plain = pl.pallas_call(
            body, name=name, grid=grid, in_specs=in_specs, out_specs=out_specs, out_shape=_out_hbm(out_shape),
            scratch_shapes=list(scratch), input_output_aliases=aliases or {}, compiler_params=params)
        return lambda *args: plain(*_in_hbm(args))
    single = not isinstance(out_shape, (list, tuple))
    out_specs_l = [out_specs] if single else list(out_specs)
    out_shape_l = [out_shape] if single else list(out_shape)
    n_in, n_out, n_sc = len(in_specs), len(out_shape_l), len(scratch)
    k_in, k_out = len(carry.ins), len(carry.outs)
    nsteps = int(np.prod(grid))
    assert nsteps >= 3

    def wrapped(*refs):
        ins, cins = refs[:n_in], refs[n_in:n_in + k_in]
        o0 = n_in + k_in
        outs, couts = refs[o0:o0 + n_out], refs[o0 + n_out:o0 + n_out + k_out]
        s0 = o0 + n_out + k_out
        sc, csc = refs[s0:s0 + n_sc], refs[s0 + n_sc:]
        step = pl.program_id(0)
        for ax in range(1, len(grid)):
            step = step * grid[ax] + pl.program_id(ax)
        start, mid, end = carry.phases(cins, couts, csc)

        @pl.when(step == 0)
        def _():
            _peer_barrier(carry.peers)
            start()

        body(*ins, *outs, *sc)
        if mid is not None:
            pl.when(step == nsteps - 2)(mid)
        pl.when(step == nsteps - 1)(end)

    all_aliases = dict(aliases or {})
    all_aliases.update({n_in + i: n_out + o for i, o in carry.aliases.items()})
    call = pl.pallas_call(
        wrapped, name=name, grid=grid, in_specs=list(in_specs) + [ANY] * k_in,
        out_specs=out_specs_l + [ANY] * k_out, out_shape=_out_hbm(out_shape_l + list(carry.outs)),
        scratch_shapes=list(scratch) + list(carry.scratch), input_output_aliases=all_aliases,
        compiler_params=pltpu.CompilerParams(dimension_semantics=("arbitrary",) * len(grid),
                                             vmem_limit_bytes=vmem_mb << 20,
                                             collective_id=BARRIER_IDS[carry.peers]))

    def run(*args):
        res = call(*_in_hbm(args), *carry.ins)
        core = res[:n_out]
        return (core[0] if single else list(core)), list(res[n_out:])

    return run


def _full(shape):
    nd = len(shape)
    return pl.BlockSpec(shape, lambda *_: (0,) * nd)


ANY = pl.BlockSpec(memory_space=pl.ANY)


NT_IN = NCOL // 256


def _src_block(j):
    return j + jnp.where(j < 4, 2, jnp.where(j < 6, 3, jnp.where(j < 8, -6, jnp.where(
        j < 16, 5, jnp.where(j < 20, -7, -14)))))


def _mm_in(h, wt, tm, carry=None):
    tt = h.shape[0]

    def body(h_ref, w_ref, o_ref):
        o_ref[...] = _bdot(h_ref[...], w_ref[...], NT)

    return _pcall(body, name="mm_in", grid=(tt // tm, NT_IN),
                  in_specs=[pl.BlockSpec((tm, D), lambda i, j: (i, 0)),
                            pl.BlockSpec((256, D), lambda i, j: (_src_block(j), 0))],
                  out_specs=pl.BlockSpec((tm, 256), lambda i, j: (i, j)),
                  out_shape=jax.ShapeDtypeStruct((tt, NCOL), F32), carry=carry)(h, wt)


def _mm_dh(dp, wt, tm, carry=None):
    tt = dp.shape[0]
    per, ng = 3, NT_IN // 3

    def body(d_ref, w0, w1, w2, o_ref, acc):
        kk = pl.program_id(1)

        @pl.when(kk == 0)
        def _():
            acc[...] = jnp.zeros_like(acc)

        acc[...] += (_bdot(d_ref[:, 0:256], w0[...]) + _bdot(d_ref[:, 256:512], w1[...])
                     + _bdot(d_ref[:, 512:768], w2[...]))

        @pl.when(kk == ng - 1)
        def _():
            o_ref[...] = acc[...]

    wspecs = [pl.BlockSpec((256, D), functools.partial(lambda t, i, kk: (_src_block(per * kk + t), 0), t))
              for t in range(per)]
    return _pcall(body, name="mm_dh", grid=(tt // tm, ng),
                  in_specs=[pl.BlockSpec((tm, per * 256), lambda i, kk: (i, kk))] + wspecs,
                  out_specs=pl.BlockSpec((tm, D), lambda i, kk: (i, 0)),
                  out_shape=jax.ShapeDtypeStruct((tt, D), F32), scratch=[pltpu.VMEM((tm, D), F32)],
                  carry=carry)(dp, wt, wt, wt)


def _mm_gin(dp, h, tk):
    tt = dp.shape[0]
    nk = tt // tk

    def body(d_ref, h_ref, o_ref, acc):
        kk = pl.program_id(1)

        @pl.when(kk == 0)
        def _():
            acc[...] = jnp.zeros_like(acc)

        acc[...] += _bdot(d_ref[...], h_ref[...], TN)

        @pl.when(kk == nk - 1)
        def _():
            o_ref[...] = acc[...].astype(BF16)

    return _pcall(body, name="mm_gin", grid=(NT_IN, nk),
                  in_specs=[pl.BlockSpec((tk, 256), lambda j, kk: (kk, j)),
                            pl.BlockSpec((tk, D), lambda j, kk: (kk, 0))],
                  out_specs=pl.BlockSpec((256, D), lambda j, kk: (_src_block(j), 0)),
                  out_shape=jax.ShapeDtypeStruct((NCOL, D), BF16), scratch=[pltpu.VMEM((256, D), F32)])(dp, h)


def _tok_specs():
    assert L == TM
    return [_full((TM, D)), pl.BlockSpec((TM, D), lambda i: (jnp.maximum(i - 1, 0), 0))]


def _mod1(ctx, x, nw, ss):
    rows = L + x.shape[0]

    def body(c_ref, x_ref, nw_ref, ss_ref, h_ref):
        t = jnp.where(pl.program_id(0) == 0, c_ref[...], x_ref[...])
        r = lax.rsqrt(jnp.mean(t * t, axis=-1, keepdims=True) + EPS)
        s = ss_ref[0]
        h_ref[...] = ((t * r * nw_ref[...]) * (1.0 + s[1:2]) + s[0:1]).astype(BF16)

    return _pcall(body, name="mod1", grid=(rows // TM,),
                  in_specs=_tok_specs() + [_full((1, D)),
                                           pl.BlockSpec((1, 2, D), lambda i: (jnp.minimum(i, 1), 0, 0))],
                  out_specs=pl.BlockSpec((TM, D), lambda i: (i, 0)),
                  out_shape=jax.ShapeDtypeStruct((rows, D), BF16))(ctx, x, nw, ss)


def _norm_bwd_rows(x, dh, nw, scale):
    r = lax.rsqrt(jnp.mean(x * x, axis=-1, keepdims=True) + EPS)
    xh = x * r
    dxh = dh * ((1.0 + scale) * nw)
    dx = r * (dxh - xh * jnp.mean(dxh * xh, axis=-1, keepdims=True))
    return dx, xh


def _out_proj_mod2(mixed, w_o, x, g1, nw2, ss2):
    s_len = x.shape[0]
    tm = 512

    def body(m_ref, w_ref, x_ref, g_ref, nw_ref, ss_ref, ao_ref, x1_ref, h_ref):
        ao = _bdot(m_ref[...], w_ref[...])
        ao_ref[...] = ao.astype(BF16)
        x1 = x_ref[...] + g_ref[...] * ao
        x1_ref[...] = x1
        r = lax.rsqrt(jnp.mean(x1 * x1, axis=-1, keepdims=True) + EPS)
        s = ss_ref[0]
        h_ref[...] = ((x1 * r * nw_ref[...]) * (1.0 + s[1:2]) + s[0:1]).astype(BF16)

    row = pl.BlockSpec((tm, D), lambda i: (i, 0))
    f = jax.ShapeDtypeStruct((s_len, D), F32)
    return _pcall(body, name="out_proj_mod2", grid=(s_len // tm,),
                  in_specs=[row, _full((D, D)), row, _full((1, D)), _full((1, D)), _full((1, 2, D))],
                  out_specs=[row, row, row],
                  out_shape=[jax.ShapeDtypeStruct((s_len, D), BF16), f,
                             jax.ShapeDtypeStruct((s_len, D), BF16)])(mixed, w_o, x, g1, nw2, ss2)


TS = 1024


def _acc_call(body, *, name, grid, in_specs, out_specs, out_shape, acc_shapes, args, carry=None):
    return _pcall(body, name=name, grid=grid, in_specs=in_specs, out_specs=out_specs, out_shape=out_shape,
                  scratch=[pltpu.VMEM(s, F32) for s in acc_shapes], carry=carry)(*args)


def _ffn_up(h2, g4, u4, carry=None):
    s_len = h2.shape[0]
    ns = g4.shape[1]

    def body(h_ref, g_ref, u_ref, a_ref, b_ref, z_ref):
        h = h_ref[...]
        a = _bdot(h, g_ref[0], NT)
        b = _bdot(h, u_ref[0], NT)
        a_ref[0] = a.astype(BF16)
        b_ref[0] = b.astype(BF16)
        z_ref[0] = (a * _sig(a) * b).astype(BF16)

    w = pl.BlockSpec((1, ns, D), lambda i, j: (j, 0, 0))
    o = pl.BlockSpec((1, TS, ns), lambda i, j: (j, i, 0))
    f = jax.ShapeDtypeStruct((4, s_len, ns), BF16)
    return _pcall(body, name="ffn_up", grid=(s_len // TS, 4),
                  in_specs=[pl.BlockSpec((TS, D), lambda i, j: (i, 0)), w, w], out_specs=[o, o, o],
                  out_shape=[f, f, jax.ShapeDtypeStruct((4, s_len, ns), BF16)], carry=carry)(h2, g4, u4)


def _ffn_down_loss(z4, dn4, x1, g2, tgt):
    _, s_len, ns = z4.shape

    def body(z_ref, w_ref, x1_ref, g_ref, t_ref, sq_ref, dx2_ref, dyb_ref, dg_ref, acc):
        i, j = pl.program_id(0), pl.program_id(1)

        @pl.when((i == 0) & (j == 0))
        def _():
            sq_ref[...] = jnp.zeros_like(sq_ref)
            dg_ref[...] = jnp.zeros_like(dg_ref)

        @pl.when(j == 0)
        def _():
            acc[...] = jnp.zeros_like(acc)

        acc[...] += _bdot(z_ref[0], w_ref[0])

        @pl.when(j == 3)
        def _():
            y_ = acc[...]
            g = g_ref[...]
            e = x1_ref[...] + g * y_ - t_ref[...]
            sq_ref[...] += jnp.sum(e * e, axis=0, keepdims=True)
            dx2 = e * (1.0 / D)
            dx2_ref[...] = dx2
            dyb_ref[...] = (g * dx2).astype(BF16)
            dg_ref[...] += jnp.sum(dx2 * y_, axis=0, keepdims=True)

    row = pl.BlockSpec((TS, D), lambda i, j: (i, 0))
    vec = _full((1, D))
    return _acc_call(body, name="ffn_down_loss", grid=(s_len // TS, 4),
                     in_specs=[pl.BlockSpec((1, TS, ns), lambda i, j: (j, i, 0)),
                               pl.BlockSpec((1, ns, D), lambda i, j: (j, 0, 0)), row, vec, row],
                     out_specs=[vec, row, row, vec],
                     out_shape=[jax.ShapeDtypeStruct((1, D), F32), jax.ShapeDtypeStruct((s_len, D), F32),
                                jax.ShapeDtypeStruct((s_len, D), BF16), jax.ShapeDtypeStruct((1, D), F32)],
                     acc_shapes=[(TS, D)], args=(z4, dn4, x1, g2, tgt))


def _ffn_dz(dyb, dn4, a4, b4):
    _, s_len, ns = a4.shape

    def body(dy_ref, w_ref, a_ref, b_ref, da_ref, db_ref):
        dz = _bdot(dy_ref[...], w_ref[0], NT)
        a = a_ref[0].astype(F32)
        s = _sig(a)
        da_ref[0] = (dz * b_ref[0].astype(F32) * (s * (1.0 + a * (1.0 - s)))).astype(BF16)
        db_ref[0] = (dz * (a * s)).astype(BF16)

    t = pl.BlockSpec((1, TS, ns), lambda i, j: (j, i, 0))
    o = jax.ShapeDtypeStruct((4, s_len, ns), BF16)
    return _pcall(body, name="ffn_dz", grid=(s_len // TS, 4),
                  in_specs=[pl.BlockSpec((TS, D), lambda i, j: (i, 0)),
                            pl.BlockSpec((1, ns, D), lambda i, j: (j, 0, 0)), t, t],
                  out_specs=[t, t], out_shape=[o, o])(dyb, dn4, a4, b4)


def _ffn_gdn(z4, dyb):
    _, s_len, ns = z4.shape
    tk = min(s_len, 2 * TS)
    nk = s_len // tk

    def body(z_ref, dy_ref, o_ref, acc):
        t = pl.program_id(1)

        @pl.when(t == 0)
        def _():
            acc[...] = jnp.zeros_like(acc)

        acc[...] += _bdot(z_ref[0], dy_ref[...], TN)

        @pl.when(t == nk - 1)
        def _():
            o_ref[0] = acc[...].astype(o_ref.dtype)

    return _acc_call(body, name="ffn_gdn", grid=(4, nk),
                     in_specs=[pl.BlockSpec((1, tk, ns), lambda j, t: (j, t, 0)),
                               pl.BlockSpec((tk, D), lambda j, t: (t, 0))],
                     out_specs=pl.BlockSpec((1, ns, D), lambda j, t: (j, 0, 0)),
                     out_shape=jax.ShapeDtypeStruct((4, ns, D), BF16), acc_shapes=[(ns, D)], args=(z4, dyb))


def _ffn_dh2(da4, db4, g4, u4, carry=None):
    _, s_len, ns = da4.shape

    def body(da_ref, db_ref, g_ref, u_ref, o_ref, acc):
        j = pl.program_id(1)

        @pl.when(j == 0)
        def _():
            acc[...] = jnp.zeros_like(acc)

        acc[...] += _bdot(da_ref[0], g_ref[0]) + _bdot(db_ref[0], u_ref[0])

        @pl.when(j == 3)
        def _():
            o_ref[...] = acc[...]

    t = pl.BlockSpec((1, TS, ns), lambda i, j: (j, i, 0))
    w = pl.BlockSpec((1, ns, D), lambda i, j: (j, 0, 0))
    return _acc_call(body, name="ffn_dh2", grid=(s_len // TS, 4), in_specs=[t, t, w, w],
                     out_specs=pl.BlockSpec((TS, D), lambda i, j: (i, 0)),
                     out_shape=jax.ShapeDtypeStruct((s_len, D), F32), acc_shapes=[(TS, D)],
                     args=(da4, db4, g4, u4), carry=carry)


def _ffn_ggu(h2, da4, db4, carry=None):
    _, s_len, ns = da4.shape
    nk = s_len // TS

    def body(h_ref, da_ref, db_ref, gg_ref, gu_ref, acc_g, acc_u):
        t = pl.program_id(1)

        @pl.when(t == 0)
        def _():
            acc_g[...] = jnp.zeros_like(acc_g)
            acc_u[...] = jnp.zeros_like(acc_u)

        h = h_ref[...]
        acc_g[...] += _bdot(da_ref[0], h, TN)
        acc_u[...] += _bdot(db_ref[0], h, TN)

        @pl.when(t == nk - 1)
        def _():
            gg_ref[0] = acc_g[...].astype(BF16)
            gu_ref[0] = acc_u[...].astype(BF16)

    d = pl.BlockSpec((1, TS, ns), lambda j, t: (j, t, 0))
    o = pl.BlockSpec((1, ns, D), lambda j, t: (j, 0, 0))
    f = jax.ShapeDtypeStruct((4, ns, D), BF16)
    return _acc_call(body, name="ffn_ggu", grid=(4, nk),
                     in_specs=[pl.BlockSpec((TS, D), lambda j, t: (t, 0)), d, d], out_specs=[o, o],
                     out_shape=[f, f], acc_shapes=[(ns, D), (ns, D)], args=(h2, da4, db4), carry=carry)


def _mod2_bwd(x1, dh2, dx2, ao, nw2, ss2, g1):
    s_len = x1.shape[0]

    def body(x1_ref, dh_ref, dx2_ref, ao_ref, nw_ref, ss_ref, g_ref,
             dx1_ref, da_ref, dss_ref, dnw_ref, dg_ref):
        i = pl.program_id(0)

        @pl.when(i == 0)
        def _():
            dss_ref[...] = jnp.zeros_like(dss_ref)
            dnw_ref[...] = jnp.zeros_like(dnw_ref)
            dg_ref[...] = jnp.zeros_like(dg_ref)

        dh = dh_ref[...]
        nw = nw_ref[...]
        scale = ss_ref[0][1:2]
        dxn, xh = _norm_bwd_rows(x1_ref[...], dh, nw, scale)
        dx1 = dx2_ref[...] + dxn
        dx1_ref[...] = dx1
        da_ref[...] = (g_ref[...] * dx1).astype(BF16)
        dg_ref[...] += jnp.sum(dx1 * ao_ref[...].astype(F32), axis=0, keepdims=True)
        dsh = jnp.sum(dh, axis=0, keepdims=True)
        dsc = jnp.sum(dh * xh * nw, axis=0, keepdims=True)
        dss_ref[...] += jnp.concatenate([dsh, dsc], axis=0)
        dnw_ref[...] += jnp.sum(dh * xh * (1.0 + scale), axis=0, keepdims=True)

    row = pl.BlockSpec((TM, D), lambda i: (i, 0))
    vec = _full((1, D))
    return _pcall(body, name="mod2_bwd", grid=(s_len // TM,),
                  in_specs=[row, row, row, row, vec, _full((1, 2, D)), vec],
                  out_specs=[row, row, _full((2, D)), vec, vec],
                  out_shape=[jax.ShapeDtypeStruct((s_len, D), F32), jax.ShapeDtypeStruct((s_len, D), BF16),
                             jax.ShapeDtypeStruct((2, D), F32), jax.ShapeDtypeStruct((1, D), F32),
                             jax.ShapeDtypeStruct((1, D), F32)])(x1, dh2, dx2, ao, nw2, ss2, g1)


def _mod1_bwd(ctx, x, dh, dx1, nw1, ss1):
    s_len = dx1.shape[0]
    tt = L + s_len

    def body(c_ref, x_ref, dh_ref, dx1_ref, nw_ref, ss_ref, dx_ref, dss_ref, dnw_ref):
        i = pl.program_id(0)
        tok = jnp.where(i == 0, c_ref[...], x_ref[...])

        @pl.when(i == 0)
        def _():
            dnw_ref[...] = jnp.zeros_like(dnw_ref)

        @pl.when(i <= 1)
        def _():
            dss_ref[...] = jnp.zeros_like(dss_ref)

        dh_ = dh_ref[...]
        nw = nw_ref[...]
        scale = ss_ref[0][1:2]
        dxn, xh = _norm_bwd_rows(tok, dh_, nw, scale)

        @pl.when(i >= 1)
        def _():
            dx_ref[...] = dx1_ref[...] + dxn

        dsh = jnp.sum(dh_, axis=0, keepdims=True)
        dsc = jnp.sum(dh_ * xh * nw, axis=0, keepdims=True)
        dss_ref[...] += jnp.concatenate([dsh, dsc], axis=0)[None]
        dnw_ref[...] += jnp.sum(dh_ * xh * (1.0 + scale), axis=0, keepdims=True)

    row = pl.BlockSpec((TM, D), lambda i: (i, 0))
    lat = pl.BlockSpec((TM, D), lambda i: (jnp.maximum(i - 1, 0), 0))
    sel = pl.BlockSpec((1, 2, D), lambda i: (jnp.minimum(i, 1), 0, 0))
    return _pcall(body, name="mod1_bwd", grid=(tt // TM,),
                  in_specs=_tok_specs() + [row, lat, _full((1, D)), sel],
                  out_specs=[lat, sel, _full((1, D))],
                  out_shape=[jax.ShapeDtypeStruct((s_len, D), F32), jax.ShapeDtypeStruct((2, 2, D), F32),
                             jax.ShapeDtypeStruct((1, D), F32)])(ctx, x, dh, dx1, nw1, ss1)


def _rows(c):
    return slice(c * CH, (c + 1) * CH)


def _chunk_masks(rev, transpose=False):
    r = lax.broadcasted_iota(jnp.int32, (TM, TM), 0)
    c = lax.broadcasted_iota(jnp.int32, (TM, TM), 1)
    same = (r // CH) == (c // CH)
    before = (c >= r) if (rev != transpose) else (c <= r)
    return same & before, same


def _chunk_scan(x, rev, transpose=False):
    r = lax.broadcasted_iota(jnp.int32, (CH, CH), 0)
    c = lax.broadcasted_iota(jnp.int32, (CH, CH), 1)
    tri = ((c >= r) if (rev != transpose) else (c <= r)).astype(F32)
    return jnp.concatenate([_dot(tri, x[_rows(ch)], prec=HI) for ch in range(x.shape[0] // CH)], axis=0)


def _chunk_total(x):
    return jnp.concatenate([jnp.broadcast_to(jnp.sum(x[_rows(ch)], axis=0, keepdims=True), (CH, x.shape[1]))
                            for ch in range(x.shape[0] // CH)], axis=0)


def _hgrn_gate(fl, qraw, lg):
    lb = 1.0 / (1.0 + jnp.exp(lg[1:2] - lg[0:1]))
    sg = _sig(fl)
    f = lb + (1.0 - lb) * sg
    q = qraw * _sig(qraw) * (HGD ** -0.5)
    return lb, sg, f, q


def _hgrn_fwd(p, lg, *, rev, carry=None, readout=None):
    tt = p.shape[0]
    nt = tt // TM
    ncht = TM // CH
    d = 1 if rev else 0

    def tile_of(s):
        return jnp.where(s == 0, 0, nt - s) if rev else s

    def body(*refs):
        if readout is None:
            f_ref, inp_ref, q_ref, lg_ref, o_ref, st_ref, state = refs
        else:
            f_ref, inp_ref, q_ref, lg_ref, oo_ref, g_ref, hw_ref, o_ref, st_ref, y_ref, state = refs
        s = pl.program_id(0)

        @pl.when(s == 0)
        def _():
            state[...] = jnp.zeros_like(state)

        _, _, f, q = _hgrn_gate(f_ref[...], q_ref[...], lg_ref[0])
        lf = jnp.log(f)
        causal, _ = _chunk_masks(rev)
        cum = _chunk_scan(lf, rev)
        tot = _chunk_total(lf)
        qd = (q * jnp.exp(cum)).astype(BF16)
        kd = ((1.0 - f) * jnp.exp(-cum)).astype(BF16)
        ke = ((1.0 - f) * jnp.exp(tot - cum)).astype(BF16)
        et = jnp.exp(tot)
        v = inp_ref[...].astype(BF16)
        order = range(ncht - 1, -1, -1) if rev else range(ncht)
        outs = []
        for h in range(4):
            sl = slice(h * HGD, (h + 1) * HGD)
            qd_, kd_, ke_, v_ = qd[:, sl], kd[:, sl], ke[:, sl], v[:, sl]
            pm = jnp.where(causal, _dot(qd_, kd_, NT), 0.0).astype(BF16)
            o_h = _dot(pm, v_)
            upd = [_dot(v_[_rows(c)], ke_[_rows(c)], TN) for c in range(ncht)]
            st = state[h]
            for c in order:
                st_ref[c, h] = st
                st = st * et[c * CH:c * CH + 1, sl] + upd[c]
            state[h] = st
            inter = [_dot(qd_[_rows(c)], st_ref[c, h].astype(BF16), NT) for c in range(ncht)]
            outs.append(o_h + jnp.concatenate(inter, axis=0))
        o_tile = jnp.concatenate(outs, axis=1)
        o_ref[...] = o_tile
        if readout is not None:
            @pl.when(tile_of(s) >= 1)
            def _():
                g = g_ref[...]
                y_ref[...] = (_head_rms(oo_ref[...] + o_tile, None, 4) * hw_ref[...] * (g * _sig(g))).astype(BF16)

    def col(cb):
        return pl.BlockSpec((TM, HGW), lambda s: (tile_of(s), cb))

    in_specs = [col(C_FB if rev else C_FF), col(C_INP), col(C_QHG), pl.BlockSpec((1, 2, HGW), lambda s: (d, 0, 0))]
    out_specs = [col(0), pl.BlockSpec((ncht, 4, HGD, HGD), lambda s: (tile_of(s), 0, 0, 0))]
    out_shape = [jax.ShapeDtypeStruct((tt, HGW), F32), jax.ShapeDtypeStruct((nt * ncht, 4, HGD, HGD), F32)]
    args = [p, p, p, lg]
    if readout is not None:
        in_specs += [col(0), col(C_GHG), _full((1, HGW))]
        args += [readout[0], p, readout[1]]
        assert rev
        out_specs.append(pl.BlockSpec((TM, HGW), lambda s: (jnp.where(s == 0, nt - 2, tile_of(s) - 1), 0)))
        out_shape.append(jax.ShapeDtypeStruct((tt - L, HGW), BF16))
    return _pcall(body, name="hgrn_fwd_rev" if rev else "hgrn_fwd", grid=(nt,), in_specs=in_specs,
                  out_specs=out_specs, out_shape=out_shape, scratch=[pltpu.VMEM((4, HGD, HGD), F32)],
                  carry=carry)(*args)


def _hgrn_bwd(p, lg, do, st, dp, prev, *, rev, carry=None):
    tt = p.shape[0]
    nt = tt // TM
    ncht = TM // CH
    d = 1 if rev else 0
    second = prev is not None

    def tile_of(s):
        return jnp.where(s == nt - 1, 0, s + 1) if rev else nt - 1 - s

    def body(*refs):
        if second:
            (f_ref, inp_ref, q_ref, lg_ref, do_ref, st_ref, dvp_ref, dqp_ref, _dp_in,
             dp_ref, dlg_ref, dstate) = refs
        else:
            (f_ref, inp_ref, q_ref, lg_ref, do_ref, st_ref, _dp_in,
             dp_ref, dv_ref, dq_ref, dlg_ref, dstate) = refs
        s = pl.program_id(0)
        tile = tile_of(s)

        @pl.when(s == 0)
        def _():
            dstate[...] = jnp.zeros_like(dstate)
            dlg_ref[...] = jnp.zeros_like(dlg_ref)

        qraw = q_ref[...]
        lb, sg, f, q = _hgrn_gate(f_ref[...], qraw, lg_ref[0])
        lf = jnp.log(f)
        causal, _ = _chunk_masks(rev)
        causal_t, _ = _chunk_masks(rev, transpose=True)
        cum = _chunk_scan(lf, rev)
        tot = _chunk_total(lf)
        ea, eb, ee, et = jnp.exp(cum), jnp.exp(-cum), jnp.exp(tot - cum), jnp.exp(tot)
        qdf, kdf, kef = q * ea, (1.0 - f) * eb, (1.0 - f) * ee
        qd, kd, ke = qdf.astype(BF16), kdf.astype(BF16), kef.astype(BF16)
        v = inp_ref[...].astype(BF16)
        dob = jnp.where(tile == 0, 0.0, do_ref[...]).astype(BF16)
        order = range(ncht) if rev else range(ncht - 1, -1, -1)
        dq_l, dk_l, dv_l, dcum_l, dtot_l = [], [], [], [], []
        for h in range(4):
            sl = slice(h * HGD, (h + 1) * HGD)
            qd_, kd_, ke_, v_, do_ = qd[:, sl], kd[:, sl], ke[:, sl], v[:, sl], dob[:, sl]
            pmt = jnp.where(causal_t, _dot(kd_, qd_, NT), 0.0).astype(BF16)
            dpm = jnp.where(causal, _dot(do_, v_, NT), 0.0).astype(BF16)
            dpmt = jnp.where(causal_t, _dot(v_, do_, NT), 0.0).astype(BF16)
            dv = _dot(pmt, do_)
            dqd = _dot(dpm, kd_)
            dkd = _dot(dpmt, qd_)
            upd = [_dot(do_[_rows(c)], qd_[_rows(c)], TN) for c in range(ncht)]
            ds = dstate[h]
            ds1 = [None] * ncht
            for c in order:
                ds1[c] = ds
                ds = ds * et[c * CH:c * CH + 1, sl] + upd[c]
            dstate[h] = ds
            dke_c, dv_c, dqd_c, dtot_c = [], [], [], []
            for c in range(ncht):
                st0 = st_ref[c, h]
                dsb = ds1[c].astype(BF16)
                dke_ = _dot(v_[_rows(c)], dsb)
                dke_c.append(dke_)
                dv_c.append(_dot(ke_[_rows(c)], dsb, NT))
                dqd_c.append(_dot(do_[_rows(c)], st0.astype(BF16)))
                dt = (jnp.sum(ds1[c] * st0, axis=0, keepdims=True) * et[c * CH:c * CH + 1, sl]
                      + jnp.sum(dke_ * kef[_rows(c), sl], axis=0, keepdims=True))
                dtot_c.append(jnp.broadcast_to(dt, (CH, HGD)))
            dke = jnp.concatenate(dke_c, axis=0)
            dqd = dqd + jnp.concatenate(dqd_c, axis=0)
            dv_l.append(dv + jnp.concatenate(dv_c, axis=0))
            dtot_l.append(jnp.concatenate(dtot_c, axis=0))
            dq_l.append(dqd * ea[:, sl])
            dk_l.append(dkd * eb[:, sl] + dke * ee[:, sl])
            dcum_l.append(dqd * qdf[:, sl] - dkd * kdf[:, sl] - dke * kef[:, sl])
        dcum = jnp.concatenate(dcum_l, axis=1)
        dlf = _chunk_scan(dcum, rev, transpose=True) + jnp.concatenate(dtot_l, axis=1)
        dq_t = jnp.concatenate(dq_l, axis=1)
        dv_t = jnp.concatenate(dv_l, axis=1)

        df = dlf / f - jnp.concatenate(dk_l, axis=1)
        dfl = df * (1.0 - lb) * sg * (1.0 - sg)
        dlb = jnp.sum(df * (1.0 - sg), axis=0, keepdims=True)
        dl0 = dlb * lb * (1.0 - lb)
        dlg_ref[...] += jnp.concatenate([dl0, -dl0], axis=0)[None]
        if second:
            sq = _sig(qraw)
            dqr = (dqp_ref[...] + dq_t) * (HGD ** -0.5) * (sq * (1.0 + qraw * (1.0 - sq)))
            dp_ref[...] = jnp.concatenate([dfl, dvp_ref[...] + dv_t, dqr], axis=1).astype(BF16)
        else:
            dp_ref[...] = dfl.astype(BF16)
            dv_ref[...] = dv_t
            dq_ref[...] = dq_t

    def col(cb):
        return pl.BlockSpec((TM, HGW), lambda s: (tile_of(s), cb))

    tok = pl.BlockSpec((TM, HGW), lambda s: (tile_of(s), 0))
    in_specs = [col(C_FB if rev else C_FF), col(C_INP), col(C_QHG),
                pl.BlockSpec((1, 2, HGW), lambda s: (d, 0, 0)),
                pl.BlockSpec((TM, HGW), lambda s: (jnp.maximum(tile_of(s) - 1, 0), 0)),
                pl.BlockSpec((ncht, 4, HGD, HGD), lambda s: (tile_of(s), 0, 0, 0))]
    args = [p, p, p, lg, do, st]
    dlg_spec = _full((1, 2, HGW))
    dlg_shape = jax.ShapeDtypeStruct((1, 2, HGW), F32)
    if second:
        in_specs += [tok, tok]
        args += [prev[0], prev[1]]
        out_specs = [pl.BlockSpec((TM, 3 * HGW), lambda s: (tile_of(s), 0)), dlg_spec]
        out_shape = [jax.ShapeDtypeStruct(dp.shape, BF16), dlg_shape]
    else:
        out_specs = [pl.BlockSpec((TM, HGW), lambda s: (tile_of(s), C_FB if rev else C_FF)), tok, tok, dlg_spec]
        out_shape = [jax.ShapeDtypeStruct(dp.shape, BF16), jax.ShapeDtypeStruct((tt, HGW), F32),
                     jax.ShapeDtypeStruct((tt, HGW), F32), dlg_shape]
    in_specs.append(ANY)
    args.append(dp)
    return _pcall(body, name="hgrn_bwd_rev" if rev else "hgrn_bwd", grid=(nt,),
                  in_specs=in_specs, out_specs=out_specs, out_shape=out_shape,
                  scratch=[pltpu.VMEM((4, HGD, HGD), F32)],
                  aliases={len(args) - 1: 0}, carry=carry)(*args)


def _head_rms(o, w, nheads):
    outs = []
    for h in range(nheads):
        oh = o[:, h * HGD:(h + 1) * HGD]
        outs.append(oh * lax.rsqrt(jnp.mean(oh * oh, axis=-1, keepdims=True) + EPS))
    return jnp.concatenate(outs, axis=1)


def _readout_bwd(o0, o1, p, hw4, dy, dp, carry=None):
    tt = o0.shape[0]
    s_len = tt - L

    def body(o0_ref, o1_ref, g_ref, w_ref, dy_ref, _dp_in, dp_ref, do_ref, dw_ref):
        i = pl.program_id(0)

        @pl.when(i == 0)
        def _():
            dw_ref[...] = jnp.zeros_like(dw_ref)
            dp_ref[...] = jnp.zeros_like(dp_ref)

        @pl.when(i >= 1)
        def _():
            o = o0_ref[...] + o1_ref[...]
            g = g_ref[...]
            w = w_ref[...]
            sg = _sig(g)
            dy_ = dy_ref[...]
            dsw = dy_ * (g * sg)
            outs, xhs = [], []
            for h in range(4):
                sl = slice(h * HGD, (h + 1) * HGD)
                oh = o[:, sl]
                r = lax.rsqrt(jnp.mean(oh * oh, axis=-1, keepdims=True) + EPS)
                xh = oh * r
                dxh = dsw[:, sl] * w[:, sl]
                outs.append(r * (dxh - xh * jnp.mean(dxh * xh, axis=-1, keepdims=True)))
                xhs.append(xh)
            xh = jnp.concatenate(xhs, axis=1)
            do_ref[...] = jnp.concatenate(outs, axis=1)
            dp_ref[...] = (dy_ * xh * w * (sg * (1.0 + g * (1.0 - sg)))).astype(BF16)
            dw_ref[...] += jnp.sum(dsw * xh, axis=0, keepdims=True)

    tok = pl.BlockSpec((TM, HGW), lambda i: (i, 0))
    lat = pl.BlockSpec((TM, HGW), lambda i: (jnp.maximum(i - 1, 0), 0))
    return _pcall(body, name="readout_bwd", grid=(tt // TM,),
                  in_specs=[tok, tok, pl.BlockSpec((TM, HGW), lambda i: (i, C_GHG)), _full((1, HGW)), lat, ANY],
                  out_specs=[pl.BlockSpec((TM, HGW), lambda i: (i, C_GHG)), lat, _full((1, HGW))],
                  out_shape=[jax.ShapeDtypeStruct(dp.shape, BF16), jax.ShapeDtypeStruct((s_len, HGW), F32),
                             jax.ShapeDtypeStruct((1, HGW), F32)],
                  aliases={5: 0}, carry=carry)(o0, o1, p, hw4, dy, dp)


def _rope_tables(s_len):
    t = np.arange(s_len)
    inv = ROPE_THETA ** (-np.arange(0, 32, 2, dtype=np.float64) / 32)
    def half(pos):
        ang = pos[:, None].astype(np.float64) * inv[None, :]
        return (np.concatenate([np.cos(ang), np.cos(ang)], 1), np.concatenate([-np.sin(ang), np.sin(ang)], 1))
    cr, sr = half(t // GRID_W)
    cc, sc = half(t % GRID_W)
    cos = np.concatenate([cr, cc, cr, cc], 1)
    sin = np.concatenate([sr, sc, sr, sc], 1)
    cos = np.concatenate([np.ones((L, 128)), cos], 0)
    sin = np.concatenate([np.zeros((L, 128)), sin], 0)
    return jnp.asarray(cos, F32), jnp.asarray(sin, F32)


def _blockdiag(n, w):
    i = np.arange(n)
    return jnp.asarray((i[:, None] // w == i[None, :] // w) / float(w), F32)


def _dup_matrix():
    m = np.zeros((128, 512), np.float32)
    for g in range(2):
        for j in range(4):
            for dd in range(HDIM):
                m[64 * g + dd, 256 * g + 64 * j + dd] = 1.0
    return m


def _head_mean(x, blockdiag):
    return _dot(x, blockdiag, prec=lax.Precision.HIGH)


def _rot(x):
    n = x.shape[1]
    lane = lax.broadcasted_iota(jnp.int32, x.shape, 1)
    return jnp.where((lane % 32) < 16, pltpu.roll(x, n - 16, 1), pltpu.roll(x, 16, 1))


def _qk_prep(p, cos, sin, qnw8, knw2, bd512, bd128, dup):
    tt = p.shape[0]

    def body(q_ref, kv_ref, cos_ref, sin_ref, qw_ref, kw_ref, b5_ref, b1_ref, dup_ref,
             qr_ref, k4_ref, v4_ref):
        cos_, sin_ = cos_ref[...], sin_ref[...]
        q = q_ref[...]
        qn = q * lax.rsqrt(_head_mean(q * q, b5_ref[...]) + EPS) * qw_ref[...]
        cos4 = jnp.concatenate([cos_] * 4, axis=1)
        sin4 = jnp.concatenate([sin_] * 4, axis=1)
        qr_ref[...] = ((qn * cos4 + _rot(qn) * sin4) * (HDIM ** -0.5)).astype(BF16)
        kv = kv_ref[...]
        k, v = kv[:, :128], kv[:, 128:]
        kn = k * lax.rsqrt(_head_mean(k * k, b1_ref[...]) + EPS) * kw_ref[...]
        kr = kn * cos_ + _rot(kn) * sin_
        k4_ref[...] = _bdot(kr, dup_ref[...]).astype(BF16)
        v4_ref[...] = _bdot(v, dup_ref[...]).astype(BF16)

    row = lambda w, cb: pl.BlockSpec((TM, w), lambda i: (i, cb))
    out = jax.ShapeDtypeStruct((tt, ATW), BF16)
    return _pcall(body, name="qk_prep", grid=(tt // TM,),
                  in_specs=[row(ATW, C_QRAW), row(256, C_KV), row(128, 0), row(128, 0),
                            _full((1, ATW)), _full((1, 128)), _full((ATW, ATW)), _full((128, 128)),
                            _full((128, ATW))],
                  out_specs=[row(ATW, 0)] * 3, out_shape=[out] * 3)(
                      p, p, cos, sin, qnw8, knw2, bd512, bd128, dup)


def _attn_masks(i, nb):
    r = lax.broadcasted_iota(jnp.int32, (4 * BLK, 3 * BLK + L), 0) % BLK
    c = lax.broadcasted_iota(jnp.int32, (4 * BLK, 3 * BLK + L), 1)
    kpos = (i - 1) * BLK + c
    loc = (jnp.abs(c - BLK - r) <= BLK) & (kpos >= 0) & (kpos < nb * BLK)
    return loc | (c >= 3 * BLK)


def _stack_mask():
    r = lax.broadcasted_iota(jnp.int32, (4 * BLK, 256), 0)
    lane = lax.broadcasted_iota(jnp.int32, (4 * BLK, 256), 1)
    return (r // BLK) == (lane // HDIM)


def _stack_heads(xg, fill=0.0):
    x4 = jnp.concatenate([xg] * 4, axis=0)
    return jnp.where(_stack_mask(), x4, jnp.full_like(x4, fill))


def _unstack_heads(x4):
    out = jnp.where(_lane_mask(0), x4[0:BLK], 0.0)
    for j in range(1, 4):
        out = out + jnp.where(_lane_mask(j), x4[j * BLK:(j + 1) * BLK], 0.0)
    return out


def _per_head_rows(vals):
    return jnp.concatenate([jnp.broadcast_to(v, (BLK, 1)) for v in vals], axis=0)


def _lane_mask(j):
    lane = lax.broadcasted_iota(jnp.int32, (1, 256), 1)
    return (lane // HDIM) == j


def _attn_specs(nb):
    blk = lambda off: pl.BlockSpec((BLK, ATW), lambda i: (jnp.clip(i + off, 0, nb - 1) + 2, 0))
    ctx = pl.BlockSpec((L, ATW), lambda i: (0, 0))
    return blk, ctx


def _attn_fwd(qr, k4, v4, sinks, carry=None):
    tt = qr.shape[0]
    s_len = tt - L
    nb = s_len // BLK

    def body(sk_ref, q_ref, kp, ko, kn, kc, vp, vo, vn, vc, y_ref, lse_ref):
        i = pl.program_id(0)
        valid = _attn_masks(i, nb)
        q = q_ref[...]
        ys, lses = [], []
        for g in range(2):
            gs = slice(256 * g, 256 * g + 256)
            kcat = jnp.concatenate([kp[:, gs], ko[:, gs], kn[:, gs], kc[:, gs]], axis=0)
            vcat = jnp.concatenate([vp[:, gs], vo[:, gs], vn[:, gs], vc[:, gs]], axis=0)
            sink4 = _per_head_rows([sk_ref[4 * g + j] for j in range(4)])
            q4 = _stack_heads(q[:, gs])
            o_parts, l_parts = [], []
            for hp in range(2):
                rows = slice(2 * BLK * hp, 2 * BLK * (hp + 1))
                sink = sink4[rows]
                s = jnp.where(valid[rows], _dot(q4[rows], kcat, NT), -1e30)
                m = jnp.maximum(jnp.max(s, axis=-1, keepdims=True), sink)
                e = jnp.exp(s - m)
                den = jnp.sum(e, axis=-1, keepdims=True) + jnp.exp(sink - m)
                o_parts.append(_bdot(e * (1.0 / den), vcat))
                l_parts.append(jnp.broadcast_to(m + jnp.log(den), (2 * BLK, 256)))
            ys.append(_unstack_heads(jnp.concatenate(o_parts, axis=0)))
            lses.append(_unstack_heads(jnp.concatenate(l_parts, axis=0)))
        y_ref[...] = jnp.concatenate(ys, axis=1).astype(BF16)
        lse_ref[...] = jnp.concatenate(lses, axis=1)

    blk, ctx = _attn_specs(nb)
    out = pl.BlockSpec((BLK, ATW), lambda i: (i, 0))
    return _pcall(body, name="attn_fwd", grid=(nb,),
                  in_specs=[pl.BlockSpec(memory_space=pltpu.SMEM), blk(0),
                            blk(-1), blk(0), blk(1), ctx, blk(-1), blk(0), blk(1), ctx],
                  out_specs=[out, out],
                  out_shape=[jax.ShapeDtypeStruct((s_len, ATW), BF16),
                             jax.ShapeDtypeStruct((s_len, ATW), F32)], carry=carry)(
                      sinks, qr, k4, k4, k4, k4, v4, v4, v4, v4)


def _attn_bwd(qr, k4, v4, sinks, y, lse, dy, carry=None):
    tt = qr.shape[0]
    s_len = tt - L
    nb = s_len // BLK

    def body(sk_ref, q_ref, kp, ko, kn, kc, vp, vo, vn, vc, y_ref, lse_ref, dy_ref,
             dq_ref, dkw_ref, dvw_ref, dkc_ref, dvc_ref, dsk_ref):
        i = pl.program_id(0)

        @pl.when(i == 0)
        def _():
            dkc_ref[...] = jnp.zeros_like(dkc_ref)
            dvc_ref[...] = jnp.zeros_like(dvc_ref)
            dsk_ref[...] = jnp.zeros_like(dsk_ref)

        valid = _attn_masks(i, nb)
        q = q_ref[...]
        dy_ = dy_ref[...]
        dly = dy_ * y_ref[...].astype(F32)
        lse_ = lse_ref[...]
        dqs = []
        for g in range(2):
            gs = slice(256 * g, 256 * g + 256)
            kcat = jnp.concatenate([kp[:, gs], ko[:, gs], kn[:, gs], kc[:, gs]], axis=0)
            vcat = jnp.concatenate([vp[:, gs], vo[:, gs], vn[:, gs], vc[:, gs]], axis=0)
            q4 = _stack_heads(q[:, gs])
            dy4 = _stack_heads(dy_[:, gs]).astype(BF16)
            lse4 = jnp.max(_stack_heads(lse_[:, gs], fill=-1e30), axis=-1, keepdims=True)
            delta = jnp.sum(_stack_heads(dly[:, gs]), axis=-1, keepdims=True)
            sink = _per_head_rows([sk_ref[4 * g + j] for j in range(4)])
            pr = jnp.where(valid, jnp.exp(_dot(q4, kcat, NT) - lse4), 0.0)
            dsb = (pr * (_dot(dy4, vcat, NT) - delta)).astype(BF16)
            dsink = jnp.exp(sink - lse4) * delta
            for j in range(4):
                dsk_ref[4 * g + j:4 * g + j + 1, :] += jnp.broadcast_to(
                    -jnp.sum(dsink[j * BLK:(j + 1) * BLK], axis=0, keepdims=True), (1, 128))
            dqs.append(_unstack_heads(_dot(dsb, kcat)))
            dkg = _dot(dsb, q4, TN)
            dvg = _dot(pr.astype(BF16), dy4, TN)
            dkw_ref[0, :, gs] = dkg[:3 * BLK]
            dvw_ref[0, :, gs] = dvg[:3 * BLK]
            dkc_ref[:, gs] += dkg[3 * BLK:]
            dvc_ref[:, gs] += dvg[3 * BLK:]
        dq_ref[...] = jnp.concatenate(dqs, axis=1)

    blk, ctx = _attn_specs(nb)
    out = pl.BlockSpec((BLK, ATW), lambda i: (i, 0))
    win = pl.BlockSpec((1, 3 * BLK, ATW), lambda i: (i, 0, 0))
    acc = _full((L, ATW))
    return _pcall(body, name="attn_bwd", grid=(nb,),
                  in_specs=[pl.BlockSpec(memory_space=pltpu.SMEM), blk(0),
                            blk(-1), blk(0), blk(1), ctx, blk(-1), blk(0), blk(1), ctx, out, out, out],
                  out_specs=[out, win, win, acc, acc, _full((8, 128))],
                  out_shape=[jax.ShapeDtypeStruct((s_len, ATW), F32),
                             jax.ShapeDtypeStruct((nb, 3 * BLK, ATW), F32),
                             jax.ShapeDtypeStruct((nb, 3 * BLK, ATW), F32),
                             jax.ShapeDtypeStruct((L, ATW), F32), jax.ShapeDtypeStruct((L, ATW), F32),
                             jax.ShapeDtypeStruct((8, 128), F32)], carry=carry)(
                      sinks, qr, k4, k4, k4, k4, v4, v4, v4, v4, y, lse, dy)


def _attn_post(p, cos, sin, qnw8, knw2, bd512, bd128, dupt, dq, dkw, dvw, dkc, dvc, dp, carry=None):
    tt = p.shape[0]
    s_len = tt - L
    nb = s_len // BLK
    nctx = L // BLK

    def body(q_ref, kv_ref, cos_ref, sin_ref, qw_ref, kw_ref, b5_ref, b1_ref, dupt_ref,
             dq_ref, kwp, kwo, kwn, vwp, vwo, vwn, dkc_ref, dvc_ref, _dp_in,
             dp_ref, dqw_ref, dkw_ref):
        t = pl.program_id(0)
        j = t - nctx

        @pl.when(t == 0)
        def _():
            dqw_ref[...] = jnp.zeros_like(dqw_ref)
            dkw_ref[...] = jnp.zeros_like(dkw_ref)

        is_lat = t >= nctx
        cos_, sin_ = cos_ref[...], sin_ref[...]
        has_p = is_lat & (j >= 1)
        has_n = is_lat & (j <= nb - 2)
        dk4 = (jnp.where(is_lat, kwo[0], dkc_ref[...]) + jnp.where(has_p, kwp[0], 0.0)
               + jnp.where(has_n, kwn[0], 0.0))
        dv4 = (jnp.where(is_lat, vwo[0], dvc_ref[...]) + jnp.where(has_p, vwp[0], 0.0)
               + jnp.where(has_n, vwn[0], 0.0))
        dkr = _dot(dk4, dupt_ref[...], prec=HI)
        dv = _dot(dv4, dupt_ref[...], prec=HI)
        kv = kv_ref[...]
        k = kv[:, :128]
        kw = kw_ref[...]
        rk = lax.rsqrt(_head_mean(k * k, b1_ref[...]) + EPS)
        xk = k * rk
        dkn = dkr * cos_ + _rot(dkr * sin_)
        dxk = dkn * kw
        dk = rk * (dxk - xk * _head_mean(dxk * xk, b1_ref[...]))
        dkw_ref[...] += jnp.sum(dkn * xk, axis=0, keepdims=True)
        q = q_ref[...]
        qw = qw_ref[...]
        rq = lax.rsqrt(_head_mean(q * q, b5_ref[...]) + EPS)
        xq = q * rq
        cos4 = jnp.concatenate([cos_] * 4, axis=1)
        sin4 = jnp.concatenate([sin_] * 4, axis=1)
        dqr = jnp.where(is_lat, dq_ref[...], 0.0) * (HDIM ** -0.5)
        dqn = dqr * cos4 + _rot(dqr * sin4)
        dxq = dqn * qw
        dqraw = rq * (dxq - xq * _head_mean(dxq * xq, b5_ref[...]))
        dqw_ref[...] += jnp.sum(dqn * xq, axis=0, keepdims=True)
        dp_ref[...] = jnp.concatenate([dqraw, dk, dv], axis=1).astype(BF16)

    row = lambda w, cb: pl.BlockSpec((BLK, w), lambda t: (t, cb))
    lat = pl.BlockSpec((BLK, ATW), lambda t: (jnp.maximum(t - nctx, 0), 0))

    def part(off):
        return pl.BlockSpec((1, BLK, ATW), lambda t: (jnp.clip(t - nctx + off, 0, nb - 1), 1 - off, 0))

    cacc = pl.BlockSpec((BLK, ATW), lambda t: (jnp.minimum(t, nctx - 1), 0))
    return _pcall(body, name="attn_post", grid=(tt // BLK,),
                  in_specs=[row(ATW, C_QRAW), row(256, C_KV), row(128, 0), row(128, 0),
                            _full((1, ATW)), _full((1, 128)), _full((ATW, ATW)), _full((128, 128)),
                            _full((ATW, 128)), lat, part(-1), part(0), part(1), part(-1), part(0), part(1),
                            cacc, cacc, ANY],
                  out_specs=[pl.BlockSpec((BLK, 768), lambda t: (t, C_QKV)), _full((1, ATW)), _full((1, 128))],
                  out_shape=[jax.ShapeDtypeStruct(dp.shape, BF16), jax.ShapeDtypeStruct((1, ATW), F32),
                             jax.ShapeDtypeStruct((1, 128), F32)],
                  aliases={18: 0}, carry=carry)(p, p, cos, sin, qnw8, knw2, bd512, bd128, dupt,
                                   dq, dkw, dkw, dkw, dvw, dvw, dvw, dkc, dvc, dp)


def _branch_merge(y_hg, y_at, bh4, ba4, p):
    s_len = y_hg.shape[0]

    def body(yh_ref, ya_ref, bh_ref, ba_ref, gh_ref, ga_ref, ah_ref, aa_ref, m_ref):
        yh, ya = yh_ref[...], ya_ref[...]
        ah = jnp.concatenate([_bdot(yh, bh_ref[j]) for j in range(4)], axis=1)
        aa = jnp.concatenate([_bdot(ya, ba_ref[j]) for j in range(4)], axis=1)
        ah_ref[...] = ah.astype(BF16)
        aa_ref[...] = aa.astype(BF16)
        m_ref[...] = (_sig(gh_ref[...]) * ah + _sig(ga_ref[...]) * aa).astype(BF16)

    row = pl.BlockSpec((TM, D), lambda i: (i, 0))
    y = pl.BlockSpec((TM, HGW), lambda i: (i, 0))
    f = jax.ShapeDtypeStruct((s_len, D), BF16)
    return _pcall(body, name="branch_merge", grid=(s_len // TM,),
                  in_specs=[y, y, _full(bh4.shape), _full(ba4.shape),
                            pl.BlockSpec((TM, D), lambda i: (i + 1, 2)), pl.BlockSpec((TM, D), lambda i: (i + 1, 3))],
                  out_specs=[row, row, row],
                  out_shape=[f, f, jax.ShapeDtypeStruct((s_len, D), BF16)])(y_hg, y_at, bh4, ba4, p, p)


def _branch_bwd(dmh, dma, bh4, ba4, y_hg, y_at):
    s_len = dmh.shape[0]
    nk = s_len // TS
    ns = D // 4

    def body(dh_ref, da_ref, bh_ref, ba_ref, yh_ref, ya_ref, dyh_ref, dya_ref, gh_ref, ga_ref, acc_h, acc_a):
        t = pl.program_id(0)

        @pl.when(t == 0)
        def _():
            acc_h[...] = jnp.zeros_like(acc_h)
            acc_a[...] = jnp.zeros_like(acc_a)

        for d_ref, w_ref, y_ref, dy_ref, acc in ((dh_ref, bh_ref, yh_ref, dyh_ref, acc_h),
                                                 (da_ref, ba_ref, ya_ref, dya_ref, acc_a)):
            y = y_ref[...]
            dy = jnp.zeros((TS, HGW), F32)
            for j in range(4):
                dj = d_ref[:, j * ns:(j + 1) * ns]
                dy = dy + _bdot(dj, w_ref[j], NT)
                acc[j] += _bdot(y, dj, TN)
            dy_ref[...] = dy

        @pl.when(t == nk - 1)
        def _():
            gh_ref[...] = acc_h[...].astype(BF16)
            ga_ref[...] = acc_a[...].astype(BF16)

    dm = pl.BlockSpec((TS, D), lambda t: (t, 0))
    y = pl.BlockSpec((TS, HGW), lambda t: (t, 0))
    w = _full(bh4.shape)
    fy = jax.ShapeDtypeStruct((s_len, HGW), F32)
    gw = jax.ShapeDtypeStruct(bh4.shape, BF16)
    return _pcall(body, name="branch_bwd", grid=(nk,), in_specs=[dm, dm, w, w, y, y],
                  out_specs=[y, y, w, w], out_shape=[fy, fy, gw, gw],
                  scratch=[pltpu.VMEM(bh4.shape, F32)] * 2)(dmh, dma, bh4, ba4, y_hg, y_at)


def _merge_bwd(dattn, w_o, mixed, ah, aa, p, carry=None):
    tt = p.shape[0]
    s_len = tt - L
    nt = tt // TM

    def body(da_ref, wo_ref, mx_ref, ah_ref, aa_ref, gh_ref, ga_ref, dp_ref, dmh_ref, dma_ref, go_ref, acc):
        i = pl.program_id(0)

        @pl.when(i == 0)
        def _():
            dp_ref[...] = jnp.zeros_like(dp_ref)
            acc[...] = jnp.zeros_like(acc)

        @pl.when(i >= 1)
        def _():
            da = da_ref[...]
            acc[...] += _bdot(mx_ref[...], da, TN)
            dm_ = _bdot(da, wo_ref[...], NT)
            sh, sa = _sig(gh_ref[...]), _sig(ga_ref[...])
            dp_ref[...] = jnp.concatenate([dm_ * ah_ref[...].astype(F32) * sh * (1.0 - sh),
                                           dm_ * aa_ref[...].astype(F32) * sa * (1.0 - sa)], axis=1).astype(BF16)
            dmh_ref[...] = (dm_ * sh).astype(BF16)
            dma_ref[...] = (dm_ * sa).astype(BF16)

        @pl.when(i == nt - 1)
        def _():
            go_ref[...] = acc[...].astype(BF16)

    lat = pl.BlockSpec((TM, D), lambda i: (jnp.maximum(i - 1, 0), 0))
    return _pcall(body, name="merge_bwd", grid=(nt,),
                  in_specs=[lat, _full((D, D)), lat, lat, lat, pl.BlockSpec((TM, D), lambda i: (i, 2)),
                            pl.BlockSpec((TM, D), lambda i: (i, 3))],
                  out_specs=[pl.BlockSpec((TM, 2 * D), lambda i: (i, C_GATES)), lat, lat, _full((D, D))],
                  out_shape=[jax.ShapeDtypeStruct((tt, NCOL), BF16), jax.ShapeDtypeStruct((s_len, D), BF16),
                             jax.ShapeDtypeStruct((s_len, D), BF16), jax.ShapeDtypeStruct((D, D), BF16)],
                  scratch=[pltpu.VMEM((D, D), F32)], carry=carry)(dattn, w_o, mixed, ah, aa, p, p)


def _local_step(x, ctx, tgt, mod, modc, nw1, nw2, lg, hw, qnw, knw, sinks,
                w_in, wts, dist=None):
    s_len = x.shape[0]
    tt = s_len + L
    ss1 = jnp.stack([modc, mod[0:2]])
    ss2 = mod[3:5][None]
    g1, g2 = mod[2:3], mod[5:6]
    hw4 = jnp.tile(hw, (1, 4))
    qnw8 = jnp.tile(qnw, (1, 8))
    knw2 = jnp.tile(knw, (1, 2))
    cos, sin = _rope_tables(s_len)
    bd512, bd128 = _blockdiag(ATW, HDIM), _blockdiag(128, HDIM)
    dupm = _dup_matrix()
    dup, dupt = jnp.asarray(dupm, BF16), jnp.asarray(dupm.T, F32)
    tmt = tt

    def four(b):
        return b.reshape(4, 2 * b.shape[1], b.shape[2])

    def halves(g):
        return g.reshape(4, 2, g.shape[1] // 2, g.shape[2])

    h = _mod1(ctx, x, nw1, ss1)
    if dist is None:
        bh4, ba4, w_o, g4, u4, dn4 = wts
        p = _mm_in(h, w_in, tmt)
        o0, st0 = _hgrn_fwd(p, lg, rev=False)
        o1, st1, y_hg = _hgrn_fwd(p, lg, rev=True, readout=(o0, hw4))
    else:
        core, chip = dist
        half = wts[3].shape[1] // 2
        p, first = _mm_in(h, w_in, tmt, carry=_carry_join(_carry_gather(list(wts[0:3])),
                                                          _carry_gather([wts[3]], rows=[(0, half)])))
        (o0, st0), (g8,) = _hgrn_fwd(p, lg, rev=False, carry=_carry_gather([first[3]], rows=[(half, half)]))
        (o1, st1, y_hg), (dn8a,) = _hgrn_fwd(p, lg, rev=True, readout=(o0, hw4),
                                             carry=_carry_gather([wts[5]], rows=[(0, half)]))
        bh4, ba4, w_o, g4 = four(first[0]), four(first[1]), four(first[2]).reshape(D, D), four(g8)
    qr, k4, v4 = _qk_prep(p, cos, sin, qnw8, knw2, bd512, bd128, dup)
    if dist is None:
        y_at, lse = _attn_fwd(qr, k4, v4, sinks)
    else:
        (y_at, lse), (u8, dn8) = _attn_fwd(qr, k4, v4, sinks,
                                           carry=_carry_gather([wts[4], dn8a], rows=[None, (half, half)]))
        u4, dn4 = four(u8), four(dn8)
    ah, aa, mixed = _branch_merge(y_hg, y_at, bh4, ba4, p)
    ao, x1, h2 = _out_proj_mod2(mixed, w_o, x, g1, nw2, ss2)
    a4, b4, z4 = _ffn_up(h2, g4, u4)
    sq, dx2, dyb, dg2 = _ffn_down_loss(z4, dn4, x1, g2, tgt)

    da4, db4 = _ffn_dz(dyb, dn4, a4, b4)
    g_dn = _ffn_gdn(z4, dyb)
    if dist is None:
        dh2 = _ffn_dh2(da4, db4, g4, u4)
    else:
        dn_units = [halves(g_dn)]
        dh2, dn_recv = _ffn_dh2(da4, db4, g4, u4, carry=_carry_pairx(dn_units))
        dn_pairs = _rs_pair_add(dn_units, dn_recv, core)
    if dist is None:
        g_g, g_u = _ffn_ggu(h2, da4, db4)
    else:
        (g_g, g_u), c_dn = _ffn_ggu(h2, da4, db4, carry=_carry_chipx(dn_pairs))
        red_dn = _rs_chip_add(dn_pairs, c_dn, core, chip)
    dx1, dattn, dss2, dnw2, dg1 = _mod2_bwd(x1, dh2, dx2, ao, nw2, ss2, g1)
    if dist is None:
        dp, dmh, dma, g_o = _merge_bwd(dattn, w_o, mixed, ah, aa, p)
    else:
        gu_units = [halves(g_g), halves(g_u)]
        (dp, dmh, dma, g_o), gu_recv = _merge_bwd(dattn, w_o, mixed, ah, aa, p, carry=_carry_pairx(gu_units))
        ffn_pairs = list(dn_pairs) + list(_rs_pair_add(gu_units, gu_recv, core))
    dy_hg, dy_at, g_bh, g_ba = _branch_bwd(dmh, dma, bh4, ba4, y_hg, y_at)
    if dist is None:
        dp, do, dhw4 = _readout_bwd(o0, o1, p, hw4, dy_hg, dp)
        dq, dkw, dvw, dkc, dvc, dsk = _attn_bwd(qr, k4, v4, sinks, y_at, lse, dy_at)
        dp, dqnw8, dknw2 = _attn_post(p, cos, sin, qnw8, knw2, bd512, bd128, dupt, dq, dkw, dvw, dkc, dvc, dp)
    else:
        mix_units = [halves(g_bh), halves(g_ba), halves(g_o.reshape(4, D // 4, D))]
        (dp, do, dhw4), mix_recv = _readout_bwd(o0, o1, p, hw4, dy_hg, dp, carry=_carry_pairx(mix_units))
        mix_pairs = _rs_pair_add(mix_units, mix_recv, core)
        (dq, dkw, dvw, dkc, dvc, dsk), bwd = _attn_bwd(
            qr, k4, v4, sinks, y_at, lse, dy_at,
            carry=_carry_join(_carry_chipx(ffn_pairs[1:2]), _carry_sibx(red_dn)))
        red_g = _rs_chip_add(ffn_pairs[1:2], bwd[0:1], core, chip)
        (dp, dqnw8, dknw2), post = _attn_post(
            p, cos, sin, qnw8, knw2, bd512, bd128, dupt, dq, dkw, dvw, dkc, dvc, dp, carry=_carry_sibx(red_g))
    if dist is None:
        dp, dv0, dq0, dlg0 = _hgrn_bwd(p, lg, do, st0, dp, None, rev=False)
        dp, dlg1 = _hgrn_bwd(p, lg, do, st1, dp, (dv0, dq0), rev=True)
    else:
        (dp, dv0, dq0, dlg0), c_u = _hgrn_bwd(p, lg, do, st0, dp, None, rev=False,
                                              carry=_carry_chipx(ffn_pairs[2:3]))
        red_u = _rs_chip_add(ffn_pairs[2:3], c_u, core, chip)
        (dp, dlg1), last = _hgrn_bwd(p, lg, do, st1, dp, (dv0, dq0), rev=True,
                                     carry=_carry_join(_carry_chipx(mix_pairs), _carry_sibx(red_u)))
        mix_reds = _rs_chip_add(mix_pairs, last[0:3], core, chip)
        ffn_done = bwd[1:2] + post[0:1] + last[3:4]
    g_in = _mm_gin(dp, h, tmt)
    if dist is None:
        dh = _mm_dh(dp, w_in, tmt)
        gx, dss1, dnw1 = _mod1_bwd(ctx, x, dh, dx1, nw1, ss1)
        rs = None
    else:
        in_units = [halves(g_in.reshape(4, NCOL // 4, D))]
        in_pairs = _rs_pair_add(in_units, _pair_exchange(in_units), core)
        first_rows, rest_rows = _split_rows(in_pairs[0].shape[1], IN_ROWS_WITH_DH)
        dh, both = _mm_dh(dp, w_in, tmt, carry=_carry_join(_carry_chipx(in_pairs, rows=first_rows),
                                                           _carry_sibx(mix_reds)))
        in_part, mix_done = both[0:1], both[1:4]
        gx, dss1, dnw1 = _mod1_bwd(ctx, x, dh, dx1, nw1, ss1)
        rs = dict(ffn_done=ffn_done, mix_done=mix_done, in_pairs=in_pairs, in_part=in_part, rest_rows=rest_rows)

    dmod = jnp.concatenate([dss1[1], dg1, dss2, dg2], axis=0)
    dmodc = dss1[0]
    raw = (dss1, dg1, dss2, dg2, dnw1, dnw2, dhw4, dqnw8, dknw2, dsk, dlg0, dlg1)
    small = dict(raw=raw, dmod=dmod, dmodc=dmodc, dnw1=dnw1, dnw2=dnw2,
                 dhw=dhw4.reshape(4, HGD).sum(0, keepdims=True),
                 dqnw=dqnw8.reshape(8, HDIM).sum(0, keepdims=True),
                 dknw=dknw2.reshape(2, HDIM).sum(0, keepdims=True),
                 dsinks=dsk[:, 0], dlg=jnp.concatenate([dlg0, dlg1], axis=0))
    big = dict(w_in=g_in, w_bh=g_bh, w_ba=g_ba, w_o=g_o, w_g=g_g, w_u=g_u, w_dn=g_dn)
    return sq, gx, big, small, rs


def _place():
    x, y, c = lax.axis_index("x"), lax.axis_index("y"), lax.axis_index("c")
    return x, y, c


def _gather_blocks(x_refs, out_refs, send_sems, recv_sems, local_sems):
    n = len(out_refs)
    x, y, c = _place()
    me, sibling = (x, y, c), (x, y, 1 - c)
    chips = [(1 - x, y), (x, 1 - y), (1 - x, 1 - y)]

    def slot(u, px, py, pc):
        return out_refs[u].at[4 * px + 2 * py + pc]

    def copy(u, k, block, to, src=None):
        return pltpu.make_async_remote_copy(
            src_ref=slot(u, *block) if src is None else src, dst_ref=slot(u, *block),
            send_sem=send_sems.at[u, k], recv_sem=recv_sems.at[u, k], device_id=to, device_id_type=MESH)

    mines = [pltpu.make_async_copy(x_refs[u], slot(u, *me), local_sems.at[u]) for u in range(n)]
    for cp in mines:
        cp.start()
    first = []
    for u in range(n):
        first.append(copy(u, 0, me, sibling, src=x_refs[u]))
        first += [copy(u, 1 + j, me, (*chip, c), src=x_refs[u]) for j, chip in enumerate(chips)]
    for cp in first:
        cp.start()
    passed = []
    for j, chip in enumerate(chips):
        for u in range(n):
            copy(u, 1 + j, (*chip, c), me).wait_recv()
            fwd = copy(u, 4 + j, (*chip, c), sibling)
            fwd.start()
            passed.append(fwd)
    for u in range(n):
        copy(u, 0, sibling, me).wait_recv()
    for j, chip in enumerate(chips):
        for u in range(n):
            copy(u, 4 + j, (*chip, 1 - c), me).wait_recv()
    for cp in first + passed:
        cp.wait_send()
    for cp in mines:
        cp.wait()


def _gather_sems(n):
    return [pltpu.SemaphoreType.DMA((n, 7)), pltpu.SemaphoreType.DMA((n, 7)), pltpu.SemaphoreType.DMA((n,))]


def _cast_place(ws, c, dev):
    n = len(ws)

    def body(s_ref, *refs):
        for u in range(n):
            refs[n + u][0] = refs[u][...].astype(BF16)

    in_specs, out_specs, out_shape = [], [], []
    for w in ws:
        q, cols = w.shape[0] // 4, w.shape[1]
        in_specs.append(pl.BlockSpec((q, cols), lambda i, s: (2 * s[0] + i, 0)))
        out_specs.append(pl.BlockSpec((1, q, cols), lambda i, s: (s[1], i, 0)))
        out_shape.append(jax.ShapeDtypeStruct((8, 2 * q, cols), BF16))
    return pl.pallas_call(
        body, name="cast_place",
        grid_spec=pltpu.PrefetchScalarGridSpec(num_scalar_prefetch=1, grid=(2,), in_specs=in_specs,
                                               out_specs=out_specs),
        out_shape=_out_hbm(out_shape),
        compiler_params=pltpu.CompilerParams(vmem_limit_bytes=48 << 20))(jnp.stack([c, dev]), *_in_hbm(ws))


def _gather_phases(out_refs, send_sems, recv_sems, rows=None):
    n = len(out_refs)
    x, y, c = _place()
    me, sibling = (x, y, c), (x, y, 1 - c)
    chips = [(1 - x, y), (x, 1 - y), (1 - x, 1 - y)]

    def copy(u, k, block, to):
        px, py, pc = block
        ref = out_refs[u].at[4 * px + 2 * py + pc]
        if rows is not None and rows[u] is not None:
            ref = ref.at[pl.ds(rows[u][0], rows[u][1])]
        return pltpu.make_async_remote_copy(src_ref=ref, dst_ref=ref, send_sem=send_sems.at[u, k],
                                            recv_sem=recv_sems.at[u, k], device_id=to, device_id_type=MESH)

    def start():
        for u in range(n):
            copy(u, 0, me, sibling).start()
            for j, chip in enumerate(chips):
                copy(u, 1 + j, me, (*chip, c)).start()

    def mid():
        for j, chip in enumerate(chips):
            for u in range(n):
                copy(u, 1 + j, (*chip, c), me).wait_recv()
                copy(u, 4 + j, (*chip, c), sibling).start()

    def end():
        for u in range(n):
            copy(u, 0, sibling, me).wait_recv()
        for j, chip in enumerate(chips):
            for u in range(n):
                copy(u, 4 + j, (*chip, 1 - c), me).wait_recv()
        for u in range(n):
            copy(u, 0, me, sibling).wait_send()
            for j, chip in enumerate(chips):
                copy(u, 1 + j, me, (*chip, c)).wait_send()
                copy(u, 4 + j, (*chip, c), sibling).wait_send()

    return start, mid, end


def _carry_gather(bufs, rows=None):
    n = len(bufs)
    return _Carry(bufs, [jax.ShapeDtypeStruct(b.shape, b.dtype) for b in bufs], {u: u for u in range(n)},
                  [pltpu.SemaphoreType.DMA((n, 7)), pltpu.SemaphoreType.DMA((n, 7))],
                  lambda ins, outs, sems: _gather_phases(outs, *sems, rows=rows), "both")


def _ag_small(raw, sq):
    def body(dss1, dg1, dss2, dg2, dnw1, dnw2, dhw4, dqnw8, dknw2, dsk, dlg0, dlg1, sq_ref,
             out_ref, tot_ref, blk, send_sems, recv_sems, local_sems):
        _peer_barrier("both")
        blk[...] = jnp.zeros_like(blk)
        blk[0:2, :] = dss1[1]
        blk[2:3, :] = dg1[...]
        blk[3:5, :] = dss2[...]
        blk[5:6, :] = dg2[...]
        blk[6:8, :] = dss1[0]
        blk[8:9, :] = dnw1[...]
        blk[9:10, :] = dnw2[...]
        blk[10:11, 0:HGW] = dhw4[...]
        blk[10:11, HGW:D] = dqnw8[...]
        blk[11:12, 0:128] = dknw2[...]
        blk[12:14, 0:HGW] = dlg0[0]
        blk[14:16, 0:HGW] = dlg1[0]
        blk[16:24, 0:128] = dsk[...]
        blk[24:25, :] = sq_ref[...]
        _gather_blocks([blk], [out_ref], send_sems, recv_sems, local_sems)
        acc = out_ref[0]
        for i in range(1, 8):
            acc = acc + out_ref[i]
        tot_ref[...] = acc

    vm = pl.BlockSpec(memory_space=pltpu.VMEM)
    return pl.pallas_call(
        body, name="ag_small",
        out_shape=[jax.ShapeDtypeStruct((8, 32, D), F32), jax.ShapeDtypeStruct((32, D), F32)],
        in_specs=[vm] * 13, out_specs=[vm, vm],
        scratch_shapes=[pltpu.VMEM((32, D), F32)] + _gather_sems(1),
        compiler_params=pltpu.CompilerParams(collective_id=BARRIER_IDS["both"]))(*raw, sq)


def _pairx_shapes(units):
    return [jax.ShapeDtypeStruct((4,) + g.shape[2:], g.dtype) for g in units]


def _pairx_phases(g_refs, r_refs, send_sems, recv_sems):
    n = len(g_refs)
    x, y, c = _place()
    cps = [pltpu.make_async_remote_copy(
        src_ref=g_refs[u].at[j, 1 - c], dst_ref=r_refs[u].at[j], send_sem=send_sems.at[u, j],
        recv_sem=recv_sems.at[u, j], device_id=(x, y, 1 - c), device_id_type=MESH)
        for u in range(n) for j in range(4)]

    def start():
        for cp in cps:
            cp.start()

    def end():
        for cp in cps:
            cp.wait()

    return start, None, end


def _carry_pairx(units):
    n = len(units)
    return _Carry(units, _pairx_shapes(units), {},
                  [pltpu.SemaphoreType.DMA((n, 4)), pltpu.SemaphoreType.DMA((n, 4))],
                  lambda ins, outs, sems: _pairx_phases(ins, outs, *sems), "sib")


def _pair_exchange(units):
    n = len(units)

    def body(*refs):
        _peer_barrier("sib")
        start, _, end = _pairx_phases(refs[:n], refs[n:2 * n], *refs[2 * n:])
        start()
        end()

    return pl.pallas_call(
        body, name="pair_exchange", out_shape=_pairx_shapes(units), in_specs=[ANY] * n, out_specs=[ANY] * n,
        scratch_shapes=[pltpu.SemaphoreType.DMA((n, 4))] * 2,
        compiler_params=pltpu.CompilerParams(collective_id=BARRIER_IDS["sib"]))(*units)


IN_ROWS_WITH_DH = 0.6


def _split_rows(h, share):
    first = int(h * share) // BF16_SUBLANES * BF16_SUBLANES
    return (0, first), (first, h - first)


def _rs_pair_add(units, recvs, c):
    n = len(units)

    def body(c_ref, *refs):
        for u in range(n):
            refs[2 * n + u][...] = (refs[u][0].astype(F32) + refs[n + u][...].astype(F32)).astype(BF16)

    in_specs, out_specs, out_shape = [], [], []
    for g in units:
        h, w = g.shape[2:]
        in_specs.append(pl.BlockSpec((1, 1, h, w), lambda j, cr: (j, cr[0], 0, 0)))
    for g in units:
        h, w = g.shape[2:]
        in_specs.append(pl.BlockSpec((1, h, w), lambda j, cr: (j, 0, 0)))
        out_specs.append(pl.BlockSpec((1, h, w), lambda j, cr: (j, 0, 0)))
        out_shape.append(jax.ShapeDtypeStruct((4, h, w), BF16))
    return pl.pallas_call(
        body, name="rs_pair_add",
        grid_spec=pltpu.PrefetchScalarGridSpec(num_scalar_prefetch=1, grid=(4,), in_specs=in_specs,
                                               out_specs=out_specs),
        out_shape=_out_hbm(out_shape),
        compiler_params=pltpu.CompilerParams(vmem_limit_bytes=48 << 20))(
            c.reshape(1), *_in_hbm(list(units) + list(recvs)))


def _chipx_phases(p_refs, r_refs, send_sems, recv_sems, rows=None):
    n = len(p_refs)
    x, y, c = _place()
    k = 2 * x + y

    def part(ref):
        return ref if rows is None else ref.at[pl.ds(rows[0], rows[1])]

    sends = []
    for d in range(1, 4):
        j = (k + d) % 4
        for u in range(n):
            sends.append(pltpu.make_async_remote_copy(
                src_ref=part(p_refs[u].at[j]), dst_ref=part(r_refs[u].at[k]), send_sem=send_sems.at[u, d - 1],
                recv_sem=recv_sems.at[u, d - 1], device_id=(j // 2, j % 2, c), device_id_type=MESH))

    def start():
        for cp in sends:
            cp.start()

    def end():
        for d in range(1, 4):
            src = (k + 4 - d) % 4
            for u in range(n):
                pltpu.make_async_remote_copy(
                    src_ref=part(p_refs[u].at[src]), dst_ref=part(r_refs[u].at[src]),
                    send_sem=send_sems.at[u, d - 1], recv_sem=recv_sems.at[u, d - 1], device_id=(x, y, c),
                    device_id_type=MESH).wait_recv()
        for cp in sends:
            cp.wait_send()

    return start, None, end


def _carry_chipx(pairs, rows=None, into=None):
    n = len(pairs)
    sems = [pltpu.SemaphoreType.DMA((n, 3)), pltpu.SemaphoreType.DMA((n, 3))]
    shapes = [jax.ShapeDtypeStruct(p.shape, p.dtype) for p in pairs]
    if into is None:
        return _Carry(pairs, shapes, {}, sems, lambda ins, outs, s: _chipx_phases(ins, outs, *s, rows=rows),
                      "chips")
    return _Carry(list(pairs) + list(into), shapes, {n + u: u for u in range(n)}, sems,
                  lambda ins, outs, s: _chipx_phases(ins[:n], outs, *s, rows=rows), "chips")


def _rs_chip_add(pairs, contribs, c, chip):
    n = len(pairs)

    def body(s_ref, *refs):
        for u in range(n):
            a, b, c_, d = refs[4 * u:4 * u + 4]
            refs[4 * n + u][0] = ((a[0].astype(F32) + b[0].astype(F32)) + c_[0].astype(F32)) + d[0].astype(F32)

    in_specs, out_specs, out_shape, args = [], [], [], []
    for p, r in zip(pairs, contribs):
        h, w = p.shape[1] // 2, p.shape[2]
        in_specs += [pl.BlockSpec((1, h, w), functools.partial(lambda d, i, s: ((s[1] + d) % 4, i, 0), d))
                     for d in range(4)]
        args += [p, r, r, r]
        out_specs.append(pl.BlockSpec((1, h, w), lambda i, s: (s[0], i, 0)))
        out_shape.append(jax.ShapeDtypeStruct((2, 2 * h, w), F32))
    return pl.pallas_call(
        body, name="rs_chip_add",
        grid_spec=pltpu.PrefetchScalarGridSpec(num_scalar_prefetch=1, grid=(2,), in_specs=in_specs,
                                               out_specs=out_specs),
        out_shape=_out_hbm(out_shape),
        compiler_params=pltpu.CompilerParams(vmem_limit_bytes=48 << 20))(jnp.stack([c, chip]), *_in_hbm(args))


def _rs_sibling_gather(reds, blks):
    n, k = len(reds), len(blks)
    vm = pl.BlockSpec(memory_space=pltpu.VMEM)

    def body(*refs):
        blk_in, red_out, blk_out = refs[n:n + k], refs[n + k:2 * n + k], refs[2 * n + k:2 * (n + k)]
        sems = refs[2 * (n + k):]
        _peer_barrier("both")
        start, _, end = _sibx_phases(red_out, *sems[:2])
        start()
        _gather_blocks(blk_in, blk_out, *sems[2:])
        end()

    res = pl.pallas_call(
        body, name="rs_sibling_gather",
        out_shape=[jax.ShapeDtypeStruct(r.shape, r.dtype) for r in reds]
        + [jax.ShapeDtypeStruct((8,) + b.shape, b.dtype) for b in blks],
        in_specs=[ANY] * n + [vm] * k, out_specs=[ANY] * n + [vm] * k,
        input_output_aliases={u: u for u in range(n)},
        scratch_shapes=[pltpu.SemaphoreType.DMA((n,))] * 2 + _gather_sems(k),
        compiler_params=pltpu.CompilerParams(collective_id=BARRIER_IDS["both"]))(*reds, *blks)
    return res[:n], res[n:]


def _sibx_phases(o_refs, send_sems, recv_sems):
    n = len(o_refs)
    x, y, c = _place()
    cps = [pltpu.make_async_remote_copy(
        src_ref=o_refs[u].at[c], dst_ref=o_refs[u].at[c], send_sem=send_sems.at[u], recv_sem=recv_sems.at[u],
        device_id=(x, y, 1 - c), device_id_type=MESH) for u in range(n)]

    def start():
        for cp in cps:
            cp.start()

    def end():
        for u in range(n):
            cps[u].wait_send()
            pltpu.make_async_remote_copy(
                src_ref=o_refs[u].at[1 - c], dst_ref=o_refs[u].at[1 - c], send_sem=send_sems.at[u],
                recv_sem=recv_sems.at[u], device_id=(x, y, 1 - c), device_id_type=MESH).wait_recv()

    return start, None, end


def _carry_sibx(reds):
    n = len(reds)
    return _Carry(reds, [jax.ShapeDtypeStruct(r.shape, r.dtype) for r in reds], {u: u for u in range(n)},
                  [pltpu.SemaphoreType.DMA((n,))] * 2, lambda ins, outs, sems: _sibx_phases(outs, *sems), "sib")


def _prologue(blk, c_ctx, w, b, in8):
    n = w.shape[1]

    def body(blk_ref, cctx_ref, w_ref, b_ref, _in_in, g0_ref, c16_ref, g1_ref, in_ref, s1, r1, l1, s2, r2, s3, r3):
        _peer_barrier("both")
        x, y, c = _place()
        start, mid, end = _gather_phases([in_ref], s3, r3)
        start_mod, mid_mod, end_mod = _gather_phases([g1_ref], s2, r2)
        _gather_blocks([blk_ref], [g0_ref], s1, r1, l1)
        start()
        c16 = jnp.concatenate([g0_ref[i, 0:1, :] for i in range(8)] + [cctx_ref[...], jnp.zeros((7, D), F32)],
                              axis=0)
        c16_ref[...] = c16
        g1_ref[4 * x + 2 * y + c] = _dot(c16 * _sig(c16), w_ref[...], prec=HI) + b_ref[...]
        start_mod()
        mid()
        mid_mod()
        end()
        end_mod()

    vm = pl.BlockSpec(memory_space=pltpu.VMEM)
    return pl.pallas_call(
        body, name="prologue",
        out_shape=[jax.ShapeDtypeStruct((8, 8, D), F32), jax.ShapeDtypeStruct((16, D), F32),
                   jax.ShapeDtypeStruct((8, 16, n), F32), jax.ShapeDtypeStruct(in8.shape, in8.dtype)],
        in_specs=[vm, vm, vm, vm, ANY], out_specs=[vm, vm, vm, ANY], input_output_aliases={4: 3},
        scratch_shapes=_gather_sems(1) + [pltpu.SemaphoreType.DMA((1, 7))] * 4,
        compiler_params=pltpu.CompilerParams(vmem_limit_bytes=48 << 20,
                                             collective_id=BARRIER_IDS["both"]))(blk, c_ctx, w, b, in8)


def _ada_bwd(c16, dmod16, w, carry=None):
    n = w.shape[1]
    tn = 512

    def body(c_ref, d_ref, w_ref, gw_ref, gc_ref):
        j = pl.program_id(0)

        @pl.when(j == 0)
        def _():
            gc_ref[...] = jnp.zeros_like(gc_ref)

        cc = c_ref[...]
        dm = d_ref[...]
        gw_ref[...] = _dot(cc * _sig(cc), dm, TN, prec=HI)
        gc_ref[...] += _dot(dm, w_ref[...], NT, prec=HI)

    return _pcall(body, name="ada_bwd", grid=(n // tn,),
                  in_specs=[_full((16, D)), pl.BlockSpec((16, tn), lambda j: (0, j)),
                            pl.BlockSpec((D, tn), lambda j: (0, j))],
                  out_specs=[pl.BlockSpec((D, tn), lambda j: (0, j)), _full((16, D))],
                  out_shape=[jax.ShapeDtypeStruct((D, n), F32),
                             jax.ShapeDtypeStruct((16, D), F32)], carry=carry)(c16, dmod16, w)


def _adam_math(w, g, m, v):
    c1 = 1.0 - ADAM_B1 ** ADAM_STEP
    c2 = 1.0 - ADAM_B2 ** ADAM_STEP
    nm = ADAM_B1 * m + (1.0 - ADAM_B1) * g
    nv = ADAM_B2 * v + (1.0 - ADAM_B2) * (g * g)
    return -ADAM_LR * ((nm / c1) / (jnp.sqrt(nv / c2) + ADAM_EPS) + ADAM_WD * w), nm, nv


def _adamw_small(ws, gs, ms, vs):
    n = len(ws)

    def body(*refs):
        for u in range(n):
            d_, nm, nv = _adam_math(refs[u][...], refs[n + u][...], refs[2 * n + u][...], refs[3 * n + u][...])
            refs[4 * n + u][...] = d_
            refs[5 * n + u][...] = nm
            refs[6 * n + u][...] = nv

    specs = [_full(w.shape) for w in ws]
    shapes = [jax.ShapeDtypeStruct(w.shape, F32) for w in ws]
    out = _pcall(body, name="adamw_small", grid=(1,), in_specs=specs * 4, out_specs=specs * 3,
                 out_shape=shapes * 3)(*ws, *gs, *ms, *vs)
    return out[:n], out[n:2 * n], out[2 * n:]


def _cctx_grad(parts, c_ctx):
    def body(p_ref, c_ref, o_ref):
        acc = p_ref[0:1, :]
        for k in range(1, 4):
            acc = acc + p_ref[k:k + 1, :]
        cc = c_ref[...]
        s = _sig(cc)
        o_ref[...] = acc * (s * (1.0 + cc * (1.0 - s)))

    return _pcall(body, name="cctx_grad", grid=(1,), in_specs=[_full(parts.shape), _full((1, D))],
                  out_specs=_full((1, D)), out_shape=jax.ShapeDtypeStruct((1, D), F32))(parts, c_ctx)


ADAM_STEPS = 8


def _adamw_multi(ws, gs, ms, vs, *, name):
    n = len(ws)

    def body(*refs):
        for u in range(n):
            g = refs[n + u][...]
            refs[4 * n + u][...] = g
            refs[5 * n + u][...], refs[6 * n + u][...], refs[7 * n + u][...] = _adam_math(
                refs[u][...], g, refs[2 * n + u][...], refs[3 * n + u][...])

    specs = [pl.BlockSpec((w.shape[0] // ADAM_STEPS, w.shape[1]), lambda i: (i, 0)) for w in ws]
    shapes = [jax.ShapeDtypeStruct(w.shape, F32) for w in ws]
    out = _pcall(body, name=name, grid=(ADAM_STEPS,), in_specs=specs * 4, out_specs=specs * 4,
                 out_shape=shapes * 4)(*ws, *gs, *ms, *vs)
    return out[:n], out[n:2 * n], out[2 * n:3 * n], out[3 * n:]


def kernel(x, c, ctx, c_ctx, w_ada, b_ada, norm_mix_w, norm_ffn_w, w_in, hgrn_lb_logits, hgrn_norm_w, q_norm_w, k_norm_w, attn_sinks, w_branch_hgrn, w_branch_attn, w_out, w_ffn_gate, w_ffn_up, w_ffn_down, loss_target, m_c_ctx, m_w_ada, m_b_ada, m_norm_mix_w, m_norm_ffn_w, m_w_in, m_hgrn_lb_logits, m_hgrn_norm_w, m_q_norm_w, m_k_norm_w, m_attn_sinks, m_w_branch_hgrn, m_w_branch_attn, m_w_out, m_w_ffn_gate, m_w_ffn_up, m_w_ffn_down, v_c_ctx, v_w_ada, v_b_ada, v_norm_mix_w, v_norm_ffn_w, v_w_in, v_hgrn_lb_logits, v_hgrn_norm_w, v_q_norm_w, v_k_norm_w, v_attn_sinks, v_w_branch_hgrn, v_w_branch_attn, v_w_out, v_w_ffn_gate, v_w_ffn_up, v_w_ffn_down):
    xi, yi, ci = _place()
    chip = 2 * xi + yi
    dev = 2 * chip + ci
    s_len = x.shape[1]

    shards = [w_in[0].T, w_branch_hgrn[0], w_branch_attn[0], w_out[0], w_ffn_gate[0].T, w_ffn_up[0].T,
              w_ffn_down[0]]
    bufs = _cast_place(shards, ci, dev)

    lbrow = jnp.pad(hgrn_lb_logits.reshape(1, 512), ((0, 0), (0, D - 512)))
    blk = jnp.concatenate([c, lbrow, jnp.zeros((6, D), F32)], axis=0)
    nada = w_ada.shape[2]
    b_sh = lax.dynamic_slice(b_ada, (0, chip * nada), (1, nada))
    g0, c16, g1, in8 = _prologue(blk, c_ctx[None], w_ada[0], b_sh, bufs[0])
    lg = g0[0::2, 1, :512].reshape(4, 2, 2, 128).transpose(1, 2, 0, 3).reshape(2, 2, HGW)
    modall = g1[0::2].transpose(1, 0, 2).reshape(16, 4 * nada)
    mod = lax.dynamic_slice(modall, (dev, 0), (1, 6 * D)).reshape(6, D)
    modc = modall[8].reshape(6, D)[:2]

    sq, gx, _, small, rs = _local_step(
        x[0], ctx[0], loss_target[0], mod, modc, norm_mix_w, norm_ffn_w, lg, hgrn_norm_w, q_norm_w,
        k_norm_w, attn_sinks[0], in8.reshape(NCOL, D), bufs[1:], dist=(ci, chip))

    def whole(r):
        return r.reshape(2 * r.shape[1], r.shape[2])

    g_dn, g_g, g_u = [whole(r) for r in rs["ffn_done"]]
    g_bh, g_ba, g_o = [whole(r) for r in rs["mix_done"]]
    in_pairs = rs["in_pairs"]

    g2, tot = _ag_small(small["raw"], sq)
    loss = 0.5 * jnp.sum(tot[24]) / D
    dmodc_tot = jnp.pad(tot[6:8].reshape(1, 2 * D), ((0, 0), (0, 4 * D)))
    g_b_ada = tot[0:6].reshape(1, 6 * D) + dmodc_tot
    dmod16 = jnp.concatenate([g2[:, 0:6].reshape(8, 6 * D), dmodc_tot, jnp.zeros((7, 6 * D), F32)], axis=0)
    (g_w_ada, gc_part), in_contribs = _ada_bwd(
        c16, lax.dynamic_slice(dmod16, (0, chip * nada), (16, nada)), w_ada[0],
        carry=_carry_chipx(in_pairs, rows=rs["rest_rows"], into=rs["in_part"]))
    in_reds, (g3,) = _rs_sibling_gather(_rs_chip_add(in_pairs, in_contribs, ci, chip), [gc_part[8:16]])
    g_in = whole(in_reds[0])
    g_c_ctx = _cctx_grad(g3[0::2, 0], c_ctx[None])[0]
    g_nw1 = tot[8:9]
    g_nw2 = tot[9:10]
    g_hw = tot[10, :HGW].reshape(4, HGD).sum(0, keepdims=True)
    g_qnw = tot[10, HGW:].reshape(8, HDIM).sum(0, keepdims=True)
    g_knw = tot[11, :128].reshape(2, HDIM).sum(0, keepdims=True)
    g_sinks = tot[16:24, 0][None]
    g_lg = lax.dynamic_slice(tot[12:16, :HGW].reshape(2, 2, HGW), (0, 0, chip * 128), (2, 2, 128))

    names = ["c_ctx", "w_ada", "b_ada", "norm_mix_w", "norm_ffn_w", "w_in", "hgrn_lb_logits", "hgrn_norm_w",
             "q_norm_w", "k_norm_w", "attn_sinks", "w_branch_hgrn", "w_branch_attn", "w_out", "w_ffn_gate",
             "w_ffn_up", "w_ffn_down"]
    ws = dict(zip(names, [c_ctx, w_ada, b_ada, norm_mix_w, norm_ffn_w, w_in, hgrn_lb_logits, hgrn_norm_w,
                          q_norm_w, k_norm_w, attn_sinks, w_branch_hgrn, w_branch_attn, w_out, w_ffn_gate,
                          w_ffn_up, w_ffn_down]))
    ms = dict(zip(names, [m_c_ctx, m_w_ada, m_b_ada, m_norm_mix_w, m_norm_ffn_w, m_w_in, m_hgrn_lb_logits,
                          m_hgrn_norm_w, m_q_norm_w, m_k_norm_w, m_attn_sinks, m_w_branch_hgrn,
                          m_w_branch_attn, m_w_out, m_w_ffn_gate, m_w_ffn_up, m_w_ffn_down]))
    vs = dict(zip(names, [v_c_ctx, v_w_ada, v_b_ada, v_norm_mix_w, v_norm_ffn_w, v_w_in, v_hgrn_lb_logits,
                          v_hgrn_norm_w, v_q_norm_w, v_k_norm_w, v_attn_sinks, v_w_branch_hgrn,
                          v_w_branch_attn, v_w_out, v_w_ffn_gate, v_w_ffn_up, v_w_ffn_down]))
    transposed = ("w_in", "w_ffn_gate", "w_ffn_up")

    def view(a, n):
        return a[0].T if n in transposed else a[0]

    def unview(a, n):
        return a.T[None] if n in transposed else a[None]

    delta, new_m, new_v, grads = {}, {}, {}, {}

    def big_adamw(group, gs, name):
        g_, d_, m_, v_ = _adamw_multi([view(ws[n], n) for n in group], gs, [view(ms[n], n) for n in group],
                                      [view(vs[n], n) for n in group], name=name)
        for i, n in enumerate(group):
            grads[n], delta[n], new_m[n], new_v[n] = (unview(g_[i], n), unview(d_[i], n), unview(m_[i], n),
                                                      unview(v_[i], n))

    big_adamw(["w_ffn_down", "w_ffn_gate", "w_ffn_up", "w_out", "w_branch_hgrn", "w_branch_attn"],
              [g_dn, g_g, g_u, g_o, g_bh, g_ba], "adamw_first")
    big_adamw(["w_in", "w_ada"], [g_in, g_w_ada], "adamw_second")
    grads.update(c_ctx=g_c_ctx, b_ada=g_b_ada, norm_mix_w=g_nw1, norm_ffn_w=g_nw2, hgrn_lb_logits=g_lg,
                 hgrn_norm_w=g_hw, q_norm_w=g_qnw, k_norm_w=g_knw, attn_sinks=g_sinks)
    small_names = [n for n in names if n not in delta]

    def two_d(a):
        return a.reshape(1, -1) if a.ndim == 1 else a

    sd, sm_, sv = _adamw_small(*[[two_d(d[n]) for n in small_names] for d in (ws, grads, ms, vs)])
    for i, n in enumerate(small_names):
        for dst, src in ((delta, sd), (new_m, sm_), (new_v, sv)):
            dst[n] = src[i].reshape(ws[n].shape)
    return (loss, gx[None], *[grads[n] for n in names], *[delta[n] for n in names],
            *[new_m[n] for n in names], *[new_v[n] for n in names])
```

```python
import functools

import numpy as np
import jax
import jax.numpy as jnp
from jax import lax
from jax.experimental import pallas as pl
from jax.experimental.pallas import tpu as pltpu

F32 = jnp.float32
BF16 = jnp.bfloat16
HI = lax.Precision.HIGHEST
MESH = pl.DeviceIdType.MESH

D = 1024
L = 256
TM = 256
HGW = 512
HGD = 128
CH = 32
ATW = 512
HDIM = 64
BLK = 128
GRID_W = 64
DFF = 2816
NCOL = 5376
EPS = 1e-6
ROPE_THETA = 10000.0
BF16_SUBLANES = 16

C_FB, C_INP, C_QHG, C_FF = 0, 1, 2, 3
C_GATES = 1
C_GHG, C_QRAW = 8, 9
C_KV = 20
C_QKV = 6

ADAM_LR, ADAM_B1, ADAM_B2, ADAM_EPS, ADAM_WD, ADAM_STEP = 0.001, 0.9, 0.999, 1e-08, 0.01, 10

NN = (((1,), (0,)), ((), ()))
NT = (((1,), (1,)), ((), ()))
TN = (((0,), (0,)), ((), ()))


def _dot(a, b, dims=NN, prec=None):
    return lax.dot_general(a, b, dims, precision=prec, preferred_element_type=F32)


def _bdot(a, b, dims=NN):
    return _dot(a.astype(BF16), b.astype(BF16), dims)


def _sig(x):
    return 1.0 / (1.0 + jnp.exp(-x))


class _Carry:
    def __init__(self, ins, outs, aliases, scratch, phases, peers):
        self.ins, self.outs, self.aliases, self.scratch, self.phases = ins, outs, aliases, scratch, phases
        self.peers = peers


BARRIER_IDS = {"sib": 1, "chips": 2, "both": 3}


def _peer_barrier(kind):
    x, y, c = _place()
    peers = []
    if kind in ("sib", "both"):
        peers.append((x, y, 1 - c))
    if kind in ("chips", "both"):
        peers += [(1 - x, y, c), (x, 1 - y, c), (1 - x, 1 - y, c)]
    bar = pltpu.get_barrier_semaphore()
    for peer in peers:
        pl.semaphore_signal(bar, inc=1, device_id=peer, device_id_type=MESH)
    pl.semaphore_wait(bar, len(peers))


def _in_hbm(args):
    return [pltpu.with_memory_space_constraint(a, pltpu.HBM) for a in args]


def _out_hbm(shapes):
    if isinstance(shapes, (list, tuple)):
        return [pltpu.HBM(s.shape, s.dtype) for s in shapes]
    return pltpu.HBM(shapes.shape, shapes.dtype)


def _carry_join(a, b):
    na_in, na_out, na_sc = len(a.ins), len(a.outs), len(a.scratch)
    aliases = dict(a.aliases)
    aliases.update({na_in + i: na_out + o for i, o in b.aliases.items()})

    def phases(ins, outs, sems):
        pa = a.phases(ins[:na_in], outs[:na_out], sems[:na_sc])
        pb = b.phases(ins[na_in:], outs[na_out:], sems[na_sc:])

        def both(fa, fb):
            if fa is None and fb is None:
                return None

            def run():
                for fn in (fa, fb):
                    if fn is not None:
                        fn()
            return run

        return tuple(both(fa, fb) for fa, fb in zip(pa, pb))

    return _Carry(list(a.ins) + list(b.ins), list(a.outs) + list(b.outs), aliases,
                  list(a.scratch) + list(b.scratch), phases, a.peers if a.peers == b.peers else "both")


def _pcall(body, *, name, grid, in_specs, out_specs, out_shape, scratch=(), aliases=None, vmem_mb=48,
           carry=None):
    params = pltpu.CompilerParams(dimension_semantics=("arbitrary",) * len(grid),
                                  vmem_limit_bytes=vmem_mb << 20)
    if carry is None:
        plain = pl.pallas_call(
            body, name=name, grid=grid, in_specs=in_specs, out_specs=out_specs, out_shape=_out_hbm(out_shape),
            scratch_shapes=list(scratch), input_output_aliases=aliases or {}, compiler_params=params)
        return lambda *args: plain(*_in_hbm(args))
    single = not isinstance(out_shape, (list, tuple))
    out_specs_l = [out_specs] if single else list(out_specs)
    out_shape_l = [out_shape] if single else list(out_shape)
    n_in, n_out, n_sc = len(in_specs), len(out_shape_l), len(scratch)
    k_in, k_out = len(carry.ins), len(carry.outs)
    nsteps = int(np.prod(grid))
    assert nsteps >= 3

    def wrapped(*refs):
        ins, cins = refs[:n_in], refs[n_in:n_in + k_in]
        o0 = n_in + k_in
        outs, couts = refs[o0:o0 + n_out], refs[o0 + n_out:o0 + n_out + k_out]
        s0 = o0 + n_out + k_out
        sc, csc = refs[s0:s0 + n_sc], refs[s0 + n_sc:]
        step = pl.program_id(0)
        for ax in range(1, len(grid)):
            step = step * grid[ax] + pl.program_id(ax)
        start, mid, end = carry.phases(cins, couts, csc)

        @pl.when(step == 0)
        def _():
            _peer_barrier(carry.peers)
            start()

        body(*ins, *outs, *sc)
        if mid is not None:
            pl.when(step == nsteps - 2)(mid)
        pl.when(step == nsteps - 1)(end)

    all_aliases = dict(aliases or {})
    all_aliases.update({n_in + i: n_out + o for i, o in carry.aliases.items()})
    call = pl.pallas_call(
        wrapped, name=name, grid=grid, in_specs=list(in_specs) + [ANY] * k_in,
        out_specs=out_specs_l + [ANY] * k_out, out_shape=_out_hbm(out_shape_l + list(carry.outs)),
        scratch_shapes=list(scratch) + list(carry.scratch), input_output_aliases=all_aliases,
        compiler_params=pltpu.CompilerParams(dimension_semantics=("arbitrary",) * len(grid),
                                             vmem_limit_bytes=vmem_mb << 20,
                                             collective_id=BARRIER_IDS[carry.peers]))

    def run(*args):
        res = call(*_in_hbm(args), *_in_hbm(carry.ins))
        core = res[:n_out]
        return (core[0] if single else list(core)), list(res[n_out:])

    return run


def _full(shape):
    nd = len(shape)
    return pl.BlockSpec(shape, lambda *_: (0,) * nd)


ANY = pl.BlockSpec(memory_space=pl.ANY)


NT_IN = NCOL // 256


def _src_block(j):
    return j + jnp.where(j < 4, 2, jnp.where(j < 6, 3, jnp.where(j < 8, -6, jnp.where(
        j < 16, 5, jnp.where(j < 20, -7, -14)))))


def _mm_in(h, wt, tm, carry=None):
    tt = h.shape[0]

    def body(h_ref, w_ref, o_ref):
        o_ref[...] = _bdot(h_ref[...], w_ref[...], NT)

    return _pcall(body, name="mm_in", grid=(tt // tm, NT_IN),
                  in_specs=[pl.BlockSpec((tm, D), lambda i, j: (i, 0)),
                            pl.BlockSpec((256, D), lambda i, j: (_src_block(j), 0))],
                  out_specs=pl.BlockSpec((tm, 256), lambda i, j: (i, j)),
                  out_shape=jax.ShapeDtypeStruct((tt, NCOL), F32), carry=carry)(h, wt)


def _mm_dh(dp, wt, tm, carry=None):
    tt = dp.shape[0]
    per, ng = 3, NT_IN // 3

    def body(d_ref, w0, w1, w2, o_ref, acc):
        kk = pl.program_id(1)

        @pl.when(kk == 0)
        def _():
            acc[...] = jnp.zeros_like(acc)

        acc[...] += (_bdot(d_ref[:, 0:256], w0[...]) + _bdot(d_ref[:, 256:512], w1[...])
                     + _bdot(d_ref[:, 512:768], w2[...]))

        @pl.when(kk == ng - 1)
        def _():
            o_ref[...] = acc[...]

    wspecs = [pl.BlockSpec((256, D), functools.partial(lambda t, i, kk: (_src_block(per * kk + t), 0), t))
              for t in range(per)]
    return _pcall(body, name="mm_dh", grid=(tt // tm, ng),
                  in_specs=[pl.BlockSpec((tm, per * 256), lambda i, kk: (i, kk))] + wspecs,
                  out_specs=pl.BlockSpec((tm, D), lambda i, kk: (i, 0)),
                  out_shape=jax.ShapeDtypeStruct((tt, D), F32), scratch=[pltpu.VMEM((tm, D), F32)],
                  carry=carry)(dp, wt, wt, wt)


def _mm_gin(dp, h, tk):
    tt = dp.shape[0]
    nk = tt // tk

    def body(d_ref, h_ref, o_ref, acc):
        kk = pl.program_id(1)

        @pl.when(kk == 0)
        def _():
            acc[...] = jnp.zeros_like(acc)

        acc[...] += _bdot(d_ref[...], h_ref[...], TN)

        @pl.when(kk == nk - 1)
        def _():
            o_ref[...] = acc[...].astype(BF16)

    return _pcall(body, name="mm_gin", grid=(NT_IN, nk),
                  in_specs=[pl.BlockSpec((tk, 256), lambda j, kk: (kk, j)),
                            pl.BlockSpec((tk, D), lambda j, kk: (kk, 0))],
                  out_specs=pl.BlockSpec((256, D), lambda j, kk: (_src_block(j), 0)),
                  out_shape=jax.ShapeDtypeStruct((NCOL, D), BF16), scratch=[pltpu.VMEM((256, D), F32)])(dp, h)


def _tok_specs():
    assert L == TM
    return [_full((TM, D)), pl.BlockSpec((TM, D), lambda i: (jnp.maximum(i - 1, 0), 0))]


def _mod1(ctx, x, nw, ss):
    rows = L + x.shape[0]

    def body(c_ref, x_ref, nw_ref, ss_ref, h_ref):
        t = jnp.where(pl.program_id(0) == 0, c_ref[...], x_ref[...])
        r = lax.rsqrt(jnp.mean(t * t, axis=-1, keepdims=True) + EPS)
        s = ss_ref[0]
        h_ref[...] = ((t * r * nw_ref[...]) * (1.0 + s[1:2]) + s[0:1]).astype(BF16)

    return _pcall(body, name="mod1", grid=(rows // TM,),
                  in_specs=_tok_specs() + [_full((1, D)),
                                           pl.BlockSpec((1, 2, D), lambda i: (jnp.minimum(i, 1), 0, 0))],
                  out_specs=pl.BlockSpec((TM, D), lambda i: (i, 0)),
                  out_shape=jax.ShapeDtypeStruct((rows, D), BF16))(ctx, x, nw, ss)


def _norm_bwd_rows(x, dh, nw, scale):
    r = lax.rsqrt(jnp.mean(x * x, axis=-1, keepdims=True) + EPS)
    xh = x * r
    dxh = dh * ((1.0 + scale) * nw)
    dx = r * (dxh - xh * jnp.mean(dxh * xh, axis=-1, keepdims=True))
    return dx, xh


def _out_proj_mod2(mixed, w_o, x, g1, nw2, ss2):
    s_len = x.shape[0]
    tm = 512

    def body(m_ref, w_ref, x_ref, g_ref, nw_ref, ss_ref, ao_ref, x1_ref, h_ref):
        ao = _bdot(m_ref[...], w_ref[...])
        ao_ref[...] = ao.astype(BF16)
        x1 = x_ref[...] + g_ref[...] * ao
        x1_ref[...] = x1
        r = lax.rsqrt(jnp.mean(x1 * x1, axis=-1, keepdims=True) + EPS)
        s = ss_ref[0]
        h_ref[...] = ((x1 * r * nw_ref[...]) * (1.0 + s[1:2]) + s[0:1]).astype(BF16)

    row = pl.BlockSpec((tm, D), lambda i: (i, 0))
    f = jax.ShapeDtypeStruct((s_len, D), F32)
    return _pcall(body, name="out_proj_mod2", grid=(s_len // tm,),
                  in_specs=[row, _full((D, D)), row, _full((1, D)), _full((1, D)), _full((1, 2, D))],
                  out_specs=[row, row, row],
                  out_shape=[jax.ShapeDtypeStruct((s_len, D), BF16), f,
                             jax.ShapeDtypeStruct((s_len, D), BF16)])(mixed, w_o, x, g1, nw2, ss2)


TS = 1024


def _acc_call(body, *, name, grid, in_specs, out_specs, out_shape, acc_shapes, args, carry=None):
    return _pcall(body, name=name, grid=grid, in_specs=in_specs, out_specs=out_specs, out_shape=out_shape,
                  scratch=[pltpu.VMEM(s, F32) for s in acc_shapes], carry=carry)(*args)


def _ffn_up(h2, g4, u4, carry=None):
    s_len = h2.shape[0]
    ns = g4.shape[1]

    def body(h_ref, g_ref, u_ref, a_ref, b_ref, z_ref):
        h = h_ref[...]
        a = _bdot(h, g_ref[0], NT)
        b = _bdot(h, u_ref[0], NT)
        a_ref[0] = a.astype(BF16)
        b_ref[0] = b.astype(BF16)
        z_ref[0] = (a * _sig(a) * b).astype(BF16)

    w = pl.BlockSpec((1, ns, D), lambda i, j: (j, 0, 0))
    o = pl.BlockSpec((1, TS, ns), lambda i, j: (j, i, 0))
    f = jax.ShapeDtypeStruct((4, s_len, ns), BF16)
    return _pcall(body, name="ffn_up", grid=(s_len // TS, 4),
                  in_specs=[pl.BlockSpec((TS, D), lambda i, j: (i, 0)), w, w], out_specs=[o, o, o],
                  out_shape=[f, f, jax.ShapeDtypeStruct((4, s_len, ns), BF16)], carry=carry)(h2, g4, u4)


def _ffn_down_loss(z4, dn4, x1, g2, tgt):
    _, s_len, ns = z4.shape

    def body(z_ref, w_ref, x1_ref, g_ref, t_ref, sq_ref, dx2_ref, dyb_ref, dg_ref, acc):
        i, j = pl.program_id(0), pl.program_id(1)

        @pl.when((i == 0) & (j == 0))
        def _():
            sq_ref[...] = jnp.zeros_like(sq_ref)
            dg_ref[...] = jnp.zeros_like(dg_ref)

        @pl.when(j == 0)
        def _():
            acc[...] = jnp.zeros_like(acc)

        acc[...] += _bdot(z_ref[0], w_ref[0])

        @pl.when(j == 3)
        def _():
            y_ = acc[...]
            g = g_ref[...]
            e = x1_ref[...] + g * y_ - t_ref[...]
            sq_ref[...] += jnp.sum(e * e, axis=0, keepdims=True)
            dx2 = e * (1.0 / D)
            dx2_ref[...] = dx2
            dyb_ref[...] = (g * dx2).astype(BF16)
            dg_ref[...] += jnp.sum(dx2 * y_, axis=0, keepdims=True)

    row = pl.BlockSpec((TS, D), lambda i, j: (i, 0))
    vec = _full((1, D))
    return _acc_call(body, name="ffn_down_loss", grid=(s_len // TS, 4),
                     in_specs=[pl.BlockSpec((1, TS, ns), lambda i, j: (j, i, 0)),
                               pl.BlockSpec((1, ns, D), lambda i, j: (j, 0, 0)), row, vec, row],
                     out_specs=[vec, row, row, vec],
                     out_shape=[jax.ShapeDtypeStruct((1, D), F32), jax.ShapeDtypeStruct((s_len, D), F32),
                                jax.ShapeDtypeStruct((s_len, D), BF16), jax.ShapeDtypeStruct((1, D), F32)],
                     acc_shapes=[(TS, D)], args=(z4, dn4, x1, g2, tgt))


def _ffn_dz(dyb, dn4, a4, b4):
    _, s_len, ns = a4.shape

    def body(dy_ref, w_ref, a_ref, b_ref, da_ref, db_ref):
        dz = _bdot(dy_ref[...], w_ref[0], NT)
        a = a_ref[0].astype(F32)
        s = _sig(a)
        da_ref[0] = (dz * b_ref[0].astype(F32) * (s * (1.0 + a * (1.0 - s)))).astype(BF16)
        db_ref[0] = (dz * (a * s)).astype(BF16)

    t = pl.BlockSpec((1, TS, ns), lambda i, j: (j, i, 0))
    o = jax.ShapeDtypeStruct((4, s_len, ns), BF16)
    return _pcall(body, name="ffn_dz", grid=(s_len // TS, 4),
                  in_specs=[pl.BlockSpec((TS, D), lambda i, j: (i, 0)),
                            pl.BlockSpec((1, ns, D), lambda i, j: (j, 0, 0)), t, t],
                  out_specs=[t, t], out_shape=[o, o])(dyb, dn4, a4, b4)


def _ffn_gdn(z4, dyb):
    _, s_len, ns = z4.shape
    tk = min(s_len, 2 * TS)
    nk = s_len // tk

    def body(z_ref, dy_ref, o_ref, acc):
        t = pl.program_id(1)

        @pl.when(t == 0)
        def _():
            acc[...] = jnp.zeros_like(acc)

        acc[...] += _bdot(z_ref[0], dy_ref[...], TN)

        @pl.when(t == nk - 1)
        def _():
            o_ref[0] = acc[...].astype(o_ref.dtype)

    return _acc_call(body, name="ffn_gdn", grid=(4, nk),
                     in_specs=[pl.BlockSpec((1, tk, ns), lambda j, t: (j, t, 0)),
                               pl.BlockSpec((tk, D), lambda j, t: (t, 0))],
                     out_specs=pl.BlockSpec((1, ns, D), lambda j, t: (j, 0, 0)),
                     out_shape=jax.ShapeDtypeStruct((4, ns, D), BF16), acc_shapes=[(ns, D)], args=(z4, dyb))


def _ffn_dh2(da4, db4, g4, u4, carry=None):
    _, s_len, ns = da4.shape

    def body(da_ref, db_ref, g_ref, u_ref, o_ref, acc):
        j = pl.program_id(1)

        @pl.when(j == 0)
        def _():
            acc[...] = jnp.zeros_like(acc)

        acc[...] += _bdot(da_ref[0], g_ref[0]) + _bdot(db_ref[0], u_ref[0])

        @pl.when(j == 3)
        def _():
            o_ref[...] = acc[...]

    t = pl.BlockSpec((1, TS, ns), lambda i, j: (j, i, 0))
    w = pl.BlockSpec((1, ns, D), lambda i, j: (j, 0, 0))
    return _acc_call(body, name="ffn_dh2", grid=(s_len // TS, 4), in_specs=[t, t, w, w],
                     out_specs=pl.BlockSpec((TS, D), lambda i, j: (i, 0)),
                     out_shape=jax.ShapeDtypeStruct((s_len, D), F32), acc_shapes=[(TS, D)],
                     args=(da4, db4, g4, u4), carry=carry)


def _ffn_ggu(h2, da4, db4, carry=None):
    _, s_len, ns = da4.shape
    nk = s_len // TS

    def body(h_ref, da_ref, db_ref, gg_ref, gu_ref, acc_g, acc_u):
        t = pl.program_id(1)

        @pl.when(t == 0)
        def _():
            acc_g[...] = jnp.zeros_like(acc_g)
            acc_u[...] = jnp.zeros_like(acc_u)

        h = h_ref[...]
        acc_g[...] += _bdot(da_ref[0], h, TN)
        acc_u[...] += _bdot(db_ref[0], h, TN)

        @pl.when(t == nk - 1)
        def _():
            gg_ref[0] = acc_g[...].astype(BF16)
            gu_ref[0] = acc_u[...].astype(BF16)

    d = pl.BlockSpec((1, TS, ns), lambda j, t: (j, t, 0))
    o = pl.BlockSpec((1, ns, D), lambda j, t: (j, 0, 0))
    f = jax.ShapeDtypeStruct((4, ns, D), BF16)
    return _acc_call(body, name="ffn_ggu", grid=(4, nk),
                     in_specs=[pl.BlockSpec((TS, D), lambda j, t: (t, 0)), d, d], out_specs=[o, o],
                     out_shape=[f, f], acc_shapes=[(ns, D), (ns, D)], args=(h2, da4, db4), carry=carry)


def _mod2_bwd(x1, dh2, dx2, ao, nw2, ss2, g1):
    s_len = x1.shape[0]

    def body(x1_ref, dh_ref, dx2_ref, ao_ref, nw_ref, ss_ref, g_ref,
             dx1_ref, da_ref, dss_ref, dnw_ref, dg_ref):
        i = pl.program_id(0)

        @pl.when(i == 0)
        def _():
            dss_ref[...] = jnp.zeros_like(dss_ref)
            dnw_ref[...] = jnp.zeros_like(dnw_ref)
            dg_ref[...] = jnp.zeros_like(dg_ref)

        dh = dh_ref[...]
        nw = nw_ref[...]
        scale = ss_ref[0][1:2]
        dxn, xh = _norm_bwd_rows(x1_ref[...], dh, nw, scale)
        dx1 = dx2_ref[...] + dxn
        dx1_ref[...] = dx1
        da_ref[...] = (g_ref[...] * dx1).astype(BF16)
        dg_ref[...] += jnp.sum(dx1 * ao_ref[...].astype(F32), axis=0, keepdims=True)
        dsh = jnp.sum(dh, axis=0, keepdims=True)
        dsc = jnp.sum(dh * xh * nw, axis=0, keepdims=True)
        dss_ref[...] += jnp.concatenate([dsh, dsc], axis=0)
        dnw_ref[...] += jnp.sum(dh * xh * (1.0 + scale), axis=0, keepdims=True)

    row = pl.BlockSpec((TM, D), lambda i: (i, 0))
    vec = _full((1, D))
    return _pcall(body, name="mod2_bwd", grid=(s_len // TM,),
                  in_specs=[row, row, row, row, vec, _full((1, 2, D)), vec],
                  out_specs=[row, row, _full((2, D)), vec, vec],
                  out_shape=[jax.ShapeDtypeStruct((s_len, D), F32), jax.ShapeDtypeStruct((s_len, D), BF16),
                             jax.ShapeDtypeStruct((2, D), F32), jax.ShapeDtypeStruct((1, D), F32),
                             jax.ShapeDtypeStruct((1, D), F32)])(x1, dh2, dx2, ao, nw2, ss2, g1)


def _mod1_bwd(ctx, x, dh, dx1, nw1, ss1):
    s_len = dx1.shape[0]
    tt = L + s_len

    def body(c_ref, x_ref, dh_ref, dx1_ref, nw_ref, ss_ref, dx_ref, dss_ref, dnw_ref):
        i = pl.program_id(0)
        tok = jnp.where(i == 0, c_ref[...], x_ref[...])

        @pl.when(i == 0)
        def _():
            dnw_ref[...] = jnp.zeros_like(dnw_ref)

        @pl.when(i <= 1)
        def _():
            dss_ref[...] = jnp.zeros_like(dss_ref)

        dh_ = dh_ref[...]
        nw = nw_ref[...]
        scale = ss_ref[0][1:2]
        dxn, xh = _norm_bwd_rows(tok, dh_, nw, scale)

        @pl.when(i >= 1)
        def _():
            dx_ref[...] = dx1_ref[...] + dxn

        dsh = jnp.sum(dh_, axis=0, keepdims=True)
        dsc = jnp.sum(dh_ * xh * nw, axis=0, keepdims=True)
        dss_ref[...] += jnp.concatenate([dsh, dsc], axis=0)[None]
        dnw_ref[...] += jnp.sum(dh_ * xh * (1.0 + scale), axis=0, keepdims=True)

    row = pl.BlockSpec((TM, D), lambda i: (i, 0))
    lat = pl.BlockSpec((TM, D), lambda i: (jnp.maximum(i - 1, 0), 0))
    sel = pl.BlockSpec((1, 2, D), lambda i: (jnp.minimum(i, 1), 0, 0))
    return _pcall(body, name="mod1_bwd", grid=(tt // TM,),
                  in_specs=_tok_specs() + [row, lat, _full((1, D)), sel],
                  out_specs=[lat, sel, _full((1, D))],
                  out_shape=[jax.ShapeDtypeStruct((s_len, D), F32), jax.ShapeDtypeStruct((2, 2, D), F32),
                             jax.ShapeDtypeStruct((1, D), F32)])(ctx, x, dh, dx1, nw1, ss1)


def _rows(c):
    return slice(c * CH, (c + 1) * CH)


def _chunk_masks(rev, transpose=False):
    r = lax.broadcasted_iota(jnp.int32, (TM, TM), 0)
    c = lax.broadcasted_iota(jnp.int32, (TM, TM), 1)
    same = (r // CH) == (c // CH)
    before = (c >= r) if (rev != transpose) else (c <= r)
    return same & before, same


def _chunk_scan(x, rev, transpose=False):
    r = lax.broadcasted_iota(jnp.int32, (CH, CH), 0)
    c = lax.broadcasted_iota(jnp.int32, (CH, CH), 1)
    tri = ((c >= r) if (rev != transpose) else (c <= r)).astype(F32)
    return jnp.concatenate([_dot(tri, x[_rows(ch)], prec=HI) for ch in range(x.shape[0] // CH)], axis=0)


def _chunk_total(x):
    return jnp.concatenate([jnp.broadcast_to(jnp.sum(x[_rows(ch)], axis=0, keepdims=True), (CH, x.shape[1]))
                            for ch in range(x.shape[0] // CH)], axis=0)


def _hgrn_gate(fl, qraw, lg):
    lb = 1.0 / (1.0 + jnp.exp(lg[1:2] - lg[0:1]))
    sg = _sig(fl)
    f = lb + (1.0 - lb) * sg
    q = qraw * _sig(qraw) * (HGD ** -0.5)
    return lb, sg, f, q


def _hgrn_fwd(p, lg, *, rev, carry=None, readout=None):
    tt = p.shape[0]
    nt = tt // TM
    ncht = TM // CH
    d = 1 if rev else 0

    def tile_of(s):
        return jnp.where(s == 0, 0, nt - s) if rev else s

    def body(*refs):
        if readout is None:
            f_ref, inp_ref, q_ref, lg_ref, o_ref, st_ref, state = refs
        else:
            f_ref, inp_ref, q_ref, lg_ref, oo_ref, g_ref, hw_ref, o_ref, st_ref, y_ref, state = refs
        s = pl.program_id(0)

        @pl.when(s == 0)
        def _():
            state[...] = jnp.zeros_like(state)

        _, _, f, q = _hgrn_gate(f_ref[...], q_ref[...], lg_ref[0])
        lf = jnp.log(f)
        causal, _ = _chunk_masks(rev)
        cum = _chunk_scan(lf, rev)
        tot = _chunk_total(lf)
        qd = (q * jnp.exp(cum)).astype(BF16)
        kd = ((1.0 - f) * jnp.exp(-cum)).astype(BF16)
        ke = ((1.0 - f) * jnp.exp(tot - cum)).astype(BF16)
        et = jnp.exp(tot)
        v = inp_ref[...].astype(BF16)
        order = range(ncht - 1, -1, -1) if rev else range(ncht)
        outs = []
        for h in range(4):
            sl = slice(h * HGD, (h + 1) * HGD)
            qd_, kd_, ke_, v_ = qd[:, sl], kd[:, sl], ke[:, sl], v[:, sl]
            pm = jnp.where(causal, _dot(qd_, kd_, NT), 0.0).astype(BF16)
            o_h = _dot(pm, v_)
            upd = [_dot(v_[_rows(c)], ke_[_rows(c)], TN) for c in range(ncht)]
            st = state[h]
            for c in order:
                st_ref[c, h] = st
                st = st * et[c * CH:c * CH + 1, sl] + upd[c]
            state[h] = st
            inter = [_dot(qd_[_rows(c)], st_ref[c, h].astype(BF16), NT) for c in range(ncht)]
            outs.append(o_h + jnp.concatenate(inter, axis=0))
        o_tile = jnp.concatenate(outs, axis=1)
        o_ref[...] = o_tile
        if readout is not None:
            @pl.when(tile_of(s) >= 1)
            def _():
                g = g_ref[...]
                y_ref[...] = (_head_rms(oo_ref[...] + o_tile, None, 4) * hw_ref[...] * (g * _sig(g))).astype(BF16)

    def col(cb):
        return pl.BlockSpec((TM, HGW), lambda s: (tile_of(s), cb))

    in_specs = [col(C_FB if rev else C_FF), col(C_INP), col(C_QHG), pl.BlockSpec((1, 2, HGW), lambda s: (d, 0, 0))]
    out_specs = [col(0), pl.BlockSpec((ncht, 4, HGD, HGD), lambda s: (tile_of(s), 0, 0, 0))]
    out_shape = [jax.ShapeDtypeStruct((tt, HGW), F32), jax.ShapeDtypeStruct((nt * ncht, 4, HGD, HGD), F32)]
    args = [p, p, p, lg]
    if readout is not None:
        in_specs += [col(0), col(C_GHG), _full((1, HGW))]
        args += [readout[0], p, readout[1]]
        assert rev
        out_specs.append(pl.BlockSpec((TM, HGW), lambda s: (jnp.where(s == 0, nt - 2, tile_of(s) - 1), 0)))
        out_shape.append(jax.ShapeDtypeStruct((tt - L, HGW), BF16))
    return _pcall(body, name="hgrn_fwd_rev" if rev else "hgrn_fwd", grid=(nt,), in_specs=in_specs,
                  out_specs=out_specs, out_shape=out_shape, scratch=[pltpu.VMEM((4, HGD, HGD), F32)],
                  carry=carry)(*args)


def _hgrn_bwd(p, lg, do, st, dp, prev, *, rev, carry=None):
    tt = p.shape[0]
    nt = tt // TM
    ncht = TM // CH
    d = 1 if rev else 0
    second = prev is not None

    def tile_of(s):
        return jnp.where(s == nt - 1, 0, s + 1) if rev else nt - 1 - s

    def body(*refs):
        if second:
            (f_ref, inp_ref, q_ref, lg_ref, do_ref, st_ref, dvp_ref, dqp_ref, _dp_in,
             dp_ref, dlg_ref, dstate) = refs
        else:
            (f_ref, inp_ref, q_ref, lg_ref, do_ref, st_ref, _dp_in,
             dp_ref, dv_ref, dq_ref, dlg_ref, dstate) = refs
        s = pl.program_id(0)
        tile = tile_of(s)

        @pl.when(s == 0)
        def _():
            dstate[...] = jnp.zeros_like(dstate)
            dlg_ref[...] = jnp.zeros_like(dlg_ref)

        qraw = q_ref[...]
        lb, sg, f, q = _hgrn_gate(f_ref[...], qraw, lg_ref[0])
        lf = jnp.log(f)
        causal, _ = _chunk_masks(rev)
        causal_t, _ = _chunk_masks(rev, transpose=True)
        cum = _chunk_scan(lf, rev)
        tot = _chunk_total(lf)
        ea, eb, ee, et = jnp.exp(cum), jnp.exp(-cum), jnp.exp(tot - cum), jnp.exp(tot)
        qdf, kdf, kef = q * ea, (1.0 - f) * eb, (1.0 - f) * ee
        qd, kd, ke = qdf.astype(BF16), kdf.astype(BF16), kef.astype(BF16)
        v = inp_ref[...].astype(BF16)
        dob = jnp.where(tile == 0, 0.0, do_ref[...]).astype(BF16)
        order = range(ncht) if rev else range(ncht - 1, -1, -1)
        dq_l, dk_l, dv_l, dcum_l, dtot_l = [], [], [], [], []
        for h in range(4):
            sl = slice(h * HGD, (h + 1) * HGD)
            qd_, kd_, ke_, v_, do_ = qd[:, sl], kd[:, sl], ke[:, sl], v[:, sl], dob[:, sl]
            pmt = jnp.where(causal_t, _dot(kd_, qd_, NT), 0.0).astype(BF16)
            dpm = jnp.where(causal, _dot(do_, v_, NT), 0.0).astype(BF16)
            dpmt = jnp.where(causal_t, _dot(v_, do_, NT), 0.0).astype(BF16)
            dv = _dot(pmt, do_)
            dqd = _dot(dpm, kd_)
            dkd = _dot(dpmt, qd_)
            upd = [_dot(do_[_rows(c)], qd_[_rows(c)], TN) for c in range(ncht)]
            ds = dstate[h]
            ds1 = [None] * ncht
            for c in order:
                ds1[c] = ds
                ds = ds * et[c * CH:c * CH + 1, sl] + upd[c]
            dstate[h] = ds
            dke_c, dv_c, dqd_c, dtot_c = [], [], [], []
            for c in range(ncht):
                st0 = st_ref[c, h]
                dsb = ds1[c].astype(BF16)
                dke_ = _dot(v_[_rows(c)], dsb)
                dke_c.append(dke_)
                dv_c.append(_dot(ke_[_rows(c)], dsb, NT))
                dqd_c.append(_dot(do_[_rows(c)], st0.astype(BF16)))
                dt = (jnp.sum(ds1[c] * st0, axis=0, keepdims=True) * et[c * CH:c * CH + 1, sl]
                      + jnp.sum(dke_ * kef[_rows(c), sl], axis=0, keepdims=True))
                dtot_c.append(jnp.broadcast_to(dt, (CH, HGD)))
            dke = jnp.concatenate(dke_c, axis=0)
            dqd = dqd + jnp.concatenate(dqd_c, axis=0)
            dv_l.append(dv + jnp.concatenate(dv_c, axis=0))
            dtot_l.append(jnp.concatenate(dtot_c, axis=0))
            dq_l.append(dqd * ea[:, sl])
            dk_l.append(dkd * eb[:, sl] + dke * ee[:, sl])
            dcum_l.append(dqd * qdf[:, sl] - dkd * kdf[:, sl] - dke * kef[:, sl])
        dcum = jnp.concatenate(dcum_l, axis=1)
        dlf = _chunk_scan(dcum, rev, transpose=True) + jnp.concatenate(dtot_l, axis=1)
        dq_t = jnp.concatenate(dq_l, axis=1)
        dv_t = jnp.concatenate(dv_l, axis=1)

        df = dlf / f - jnp.concatenate(dk_l, axis=1)
        dfl = df * (1.0 - lb) * sg * (1.0 - sg)
        dlb = jnp.sum(df * (1.0 - sg), axis=0, keepdims=True)
        dl0 = dlb * lb * (1.0 - lb)
        dlg_ref[...] += jnp.concatenate([dl0, -dl0], axis=0)[None]
        if second:
            sq = _sig(qraw)
            dqr = (dqp_ref[...] + dq_t) * (HGD ** -0.5) * (sq * (1.0 + qraw * (1.0 - sq)))
            dp_ref[...] = jnp.concatenate([dfl, dvp_ref[...] + dv_t, dqr], axis=1).astype(BF16)
        else:
            dp_ref[...] = dfl.astype(BF16)
            dv_ref[...] = dv_t
            dq_ref[...] = dq_t

    def col(cb):
        return pl.BlockSpec((TM, HGW), lambda s: (tile_of(s), cb))

    tok = pl.BlockSpec((TM, HGW), lambda s: (tile_of(s), 0))
    in_specs = [col(C_FB if rev else C_FF), col(C_INP), col(C_QHG),
                pl.BlockSpec((1, 2, HGW), lambda s: (d, 0, 0)),
                pl.BlockSpec((TM, HGW), lambda s: (jnp.maximum(tile_of(s) - 1, 0), 0)),
                pl.BlockSpec((ncht, 4, HGD, HGD), lambda s: (tile_of(s), 0, 0, 0))]
    args = [p, p, p, lg, do, st]
    dlg_spec = _full((1, 2, HGW))
    dlg_shape = jax.ShapeDtypeStruct((1, 2, HGW), F32)
    if second:
        in_specs += [tok, tok]
        args += [prev[0], prev[1]]
        out_specs = [pl.BlockSpec((TM, 3 * HGW), lambda s: (tile_of(s), 0)), dlg_spec]
        out_shape = [jax.ShapeDtypeStruct(dp.shape, BF16), dlg_shape]
    else:
        out_specs = [pl.BlockSpec((TM, HGW), lambda s: (tile_of(s), C_FB if rev else C_FF)), tok, tok, dlg_spec]
        out_shape = [jax.ShapeDtypeStruct(dp.shape, BF16), jax.ShapeDtypeStruct((tt, HGW), F32),
                     jax.ShapeDtypeStruct((tt, HGW), F32), dlg_shape]
    in_specs.append(ANY)
    args.append(dp)
    return _pcall(body, name="hgrn_bwd_rev" if rev else "hgrn_bwd", grid=(nt,),
                  in_specs=in_specs, out_specs=out_specs, out_shape=out_shape,
                  scratch=[pltpu.VMEM((4, HGD, HGD), F32)],
                  aliases={len(args) - 1: 0}, carry=carry)(*args)


def _head_rms(o, w, nheads):
    outs = []
    for h in range(nheads):
        oh = o[:, h * HGD:(h + 1) * HGD]
        outs.append(oh * lax.rsqrt(jnp.mean(oh * oh, axis=-1, keepdims=True) + EPS))
    return jnp.concatenate(outs, axis=1)


def _readout_bwd(o0, o1, p, hw4, dy, dp, carry=None):
    tt = o0.shape[0]
    s_len = tt - L

    def body(o0_ref, o1_ref, g_ref, w_ref, dy_ref, _dp_in, dp_ref, do_ref, dw_ref):
        i = pl.program_id(0)

        @pl.when(i == 0)
        def _():
            dw_ref[...] = jnp.zeros_like(dw_ref)
            dp_ref[...] = jnp.zeros_like(dp_ref)

        @pl.when(i >= 1)
        def _():
            o = o0_ref[...] + o1_ref[...]
            g = g_ref[...]
            w = w_ref[...]
            sg = _sig(g)
            dy_ = dy_ref[...]
            dsw = dy_ * (g * sg)
            outs, xhs = [], []
            for h in range(4):
                sl = slice(h * HGD, (h + 1) * HGD)
                oh = o[:, sl]
                r = lax.rsqrt(jnp.mean(oh * oh, axis=-1, keepdims=True) + EPS)
                xh = oh * r
                dxh = dsw[:, sl] * w[:, sl]
                outs.append(r * (dxh - xh * jnp.mean(dxh * xh, axis=-1, keepdims=True)))
                xhs.append(xh)
            xh = jnp.concatenate(xhs, axis=1)
            do_ref[...] = jnp.concatenate(outs, axis=1)
            dp_ref[...] = (dy_ * xh * w * (sg * (1.0 + g * (1.0 - sg)))).astype(BF16)
            dw_ref[...] += jnp.sum(dsw * xh, axis=0, keepdims=True)

    tok = pl.BlockSpec((TM, HGW), lambda i: (i, 0))
    lat = pl.BlockSpec((TM, HGW), lambda i: (jnp.maximum(i - 1, 0), 0))
    return _pcall(body, name="readout_bwd", grid=(tt // TM,),
                  in_specs=[tok, tok, pl.BlockSpec((TM, HGW), lambda i: (i, C_GHG)), _full((1, HGW)), lat, ANY],
                  out_specs=[pl.BlockSpec((TM, HGW), lambda i: (i, C_GHG)), lat, _full((1, HGW))],
                  out_shape=[jax.ShapeDtypeStruct(dp.shape, BF16), jax.ShapeDtypeStruct((s_len, HGW), F32),
                             jax.ShapeDtypeStruct((1, HGW), F32)],
                  aliases={5: 0}, carry=carry)(o0, o1, p, hw4, dy, dp)


def _rope_tables(s_len):
    t = np.arange(s_len)
    inv = ROPE_THETA ** (-np.arange(0, 32, 2, dtype=np.float64) / 32)
    def half(pos):
        ang = pos[:, None].astype(np.float64) * inv[None, :]
        return (np.concatenate([np.cos(ang), np.cos(ang)], 1), np.concatenate([-np.sin(ang), np.sin(ang)], 1))
    cr, sr = half(t // GRID_W)
    cc, sc = half(t % GRID_W)
    cos = np.concatenate([cr, cc, cr, cc], 1)
    sin = np.concatenate([sr, sc, sr, sc], 1)
    cos = np.concatenate([np.ones((L, 128)), cos], 0)
    sin = np.concatenate([np.zeros((L, 128)), sin], 0)
    return jnp.asarray(cos, F32), jnp.asarray(sin, F32)


def _blockdiag(n, w):
    i = np.arange(n)
    return jnp.asarray((i[:, None] // w == i[None, :] // w) / float(w), F32)


def _dup_matrix():
    m = np.zeros((128, 512), np.float32)
    for g in range(2):
        for j in range(4):
            for dd in range(HDIM):
                m[64 * g + dd, 256 * g + 64 * j + dd] = 1.0
    return m


def _head_mean(x, blockdiag):
    return _dot(x, blockdiag, prec=lax.Precision.HIGH)


def _rot(x):
    n = x.shape[1]
    lane = lax.broadcasted_iota(jnp.int32, x.shape, 1)
    return jnp.where((lane % 32) < 16, pltpu.roll(x, n - 16, 1), pltpu.roll(x, 16, 1))


def _qk_prep(p, cos, sin, qnw8, knw2, bd512, bd128, dup):
    tt = p.shape[0]

    def body(q_ref, kv_ref, cos_ref, sin_ref, qw_ref, kw_ref, b5_ref, b1_ref, dup_ref,
             qr_ref, k4_ref, v4_ref):
        cos_, sin_ = cos_ref[...], sin_ref[...]
        q = q_ref[...]
        qn = q * lax.rsqrt(_head_mean(q * q, b5_ref[...]) + EPS) * qw_ref[...]
        cos4 = jnp.concatenate([cos_] * 4, axis=1)
        sin4 = jnp.concatenate([sin_] * 4, axis=1)
        qr_ref[...] = ((qn * cos4 + _rot(qn) * sin4) * (HDIM ** -0.5)).astype(BF16)
        kv = kv_ref[...]
        k, v = kv[:, :128], kv[:, 128:]
        kn = k * lax.rsqrt(_head_mean(k * k, b1_ref[...]) + EPS) * kw_ref[...]
        kr = kn * cos_ + _rot(kn) * sin_
        k4_ref[...] = _bdot(kr, dup_ref[...]).astype(BF16)
        v4_ref[...] = _bdot(v, dup_ref[...]).astype(BF16)

    row = lambda w, cb: pl.BlockSpec((TM, w), lambda i: (i, cb))
    out = jax.ShapeDtypeStruct((tt, ATW), BF16)
    return _pcall(body, name="qk_prep", grid=(tt // TM,),
                  in_specs=[row(ATW, C_QRAW), row(256, C_KV), row(128, 0), row(128, 0),
                            _full((1, ATW)), _full((1, 128)), _full((ATW, ATW)), _full((128, 128)),
                            _full((128, ATW))],
                  out_specs=[row(ATW, 0)] * 3, out_shape=[out] * 3)(
                      p, p, cos, sin, qnw8, knw2, bd512, bd128, dup)


def _attn_masks(i, nb):
    r = lax.broadcasted_iota(jnp.int32, (4 * BLK, 3 * BLK + L), 0) % BLK
    c = lax.broadcasted_iota(jnp.int32, (4 * BLK, 3 * BLK + L), 1)
    kpos = (i - 1) * BLK + c
    loc = (jnp.abs(c - BLK - r) <= BLK) & (kpos >= 0) & (kpos < nb * BLK)
    return loc | (c >= 3 * BLK)


def _stack_mask():
    r = lax.broadcasted_iota(jnp.int32, (4 * BLK, 256), 0)
    lane = lax.broadcasted_iota(jnp.int32, (4 * BLK, 256), 1)
    return (r // BLK) == (lane // HDIM)


def _stack_heads(xg, fill=0.0):
    x4 = jnp.concatenate([xg] * 4, axis=0)
    return jnp.where(_stack_mask(), x4, jnp.full_like(x4, fill))


def _unstack_heads(x4):
    out = jnp.where(_lane_mask(0), x4[0:BLK], 0.0)
    for j in range(1, 4):
        out = out + jnp.where(_lane_mask(j), x4[j * BLK:(j + 1) * BLK], 0.0)
    return out


def _per_head_rows(vals):
    return jnp.concatenate([jnp.broadcast_to(v, (BLK, 1)) for v in vals], axis=0)


def _lane_mask(j):
    lane = lax.broadcasted_iota(jnp.int32, (1, 256), 1)
    return (lane // HDIM) == j


def _attn_specs(nb):
    blk = lambda off: pl.BlockSpec((BLK, ATW), lambda i: (jnp.clip(i + off, 0, nb - 1) + 2, 0))
    ctx = pl.BlockSpec((L, ATW), lambda i: (0, 0))
    return blk, ctx


def _attn_fwd(qr, k4, v4, sinks, carry=None):
    tt = qr.shape[0]
    s_len = tt - L
    nb = s_len // BLK

    def body(sk_ref, q_ref, kp, ko, kn, kc, vp, vo, vn, vc, y_ref, lse_ref):
        i = pl.program_id(0)
        valid = _attn_masks(i, nb)
        q = q_ref[...]
        ys, lses = [], []
        for g in range(2):
            gs = slice(256 * g, 256 * g + 256)
            kcat = jnp.concatenate([kp[:, gs], ko[:, gs], kn[:, gs], kc[:, gs]], axis=0)
            vcat = jnp.concatenate([vp[:, gs], vo[:, gs], vn[:, gs], vc[:, gs]], axis=0)
            sink4 = _per_head_rows([sk_ref[4 * g + j] for j in range(4)])
            q4 = _stack_heads(q[:, gs])
            o_parts, l_parts = [], []
            for hp in range(2):
                rows = slice(2 * BLK * hp, 2 * BLK * (hp + 1))
                sink = sink4[rows]
                s = jnp.where(valid[rows], _dot(q4[rows], kcat, NT), -1e30)
                m = jnp.maximum(jnp.max(s, axis=-1, keepdims=True), sink)
                e = jnp.exp(s - m)
                den = jnp.sum(e, axis=-1, keepdims=True) + jnp.exp(sink - m)
                o_parts.append(_bdot(e * (1.0 / den), vcat))
                l_parts.append(jnp.broadcast_to(m + jnp.log(den), (2 * BLK, 256)))
            ys.append(_unstack_heads(jnp.concatenate(o_parts, axis=0)))
            lses.append(_unstack_heads(jnp.concatenate(l_parts, axis=0)))
        y_ref[...] = jnp.concatenate(ys, axis=1).astype(BF16)
        lse_ref[...] = jnp.concatenate(lses, axis=1)

    blk, ctx = _attn_specs(nb)
    out = pl.BlockSpec((BLK, ATW), lambda i: (i, 0))
    return _pcall(body, name="attn_fwd", grid=(nb,),
                  in_specs=[pl.BlockSpec(memory_space=pltpu.SMEM), blk(0),
                            blk(-1), blk(0), blk(1), ctx, blk(-1), blk(0), blk(1), ctx],
                  out_specs=[out, out],
                  out_shape=[jax.ShapeDtypeStruct((s_len, ATW), BF16),
                             jax.ShapeDtypeStruct((s_len, ATW), F32)], carry=carry)(
                      sinks, qr, k4, k4, k4, k4, v4, v4, v4, v4)


def _attn_bwd(qr, k4, v4, sinks, y, lse, dy, carry=None):
    tt = qr.shape[0]
    s_len = tt - L
    nb = s_len // BLK

    def body(sk_ref, q_ref, kp, ko, kn, kc, vp, vo, vn, vc, y_ref, lse_ref, dy_ref,
             dq_ref, dkw_ref, dvw_ref, dkc_ref, dvc_ref, dsk_ref):
        i = pl.program_id(0)

        @pl.when(i == 0)
        def _():
            dkc_ref[...] = jnp.zeros_like(dkc_ref)
            dvc_ref[...] = jnp.zeros_like(dvc_ref)
            dsk_ref[...] = jnp.zeros_like(dsk_ref)

        valid = _attn_masks(i, nb)
        q = q_ref[...]
        dy_ = dy_ref[...]
        dly = dy_ * y_ref[...].astype(F32)
        lse_ = lse_ref[...]
        dqs = []
        for g in range(2):
            gs = slice(256 * g, 256 * g + 256)
            kcat = jnp.concatenate([kp[:, gs], ko[:, gs], kn[:, gs], kc[:, gs]], axis=0)
            vcat = jnp.concatenate([vp[:, gs], vo[:, gs], vn[:, gs], vc[:, gs]], axis=0)
            q4 = _stack_heads(q[:, gs])
            dy4 = _stack_heads(dy_[:, gs]).astype(BF16)
            lse4 = jnp.max(_stack_heads(lse_[:, gs], fill=-1e30), axis=-1, keepdims=True)
            delta = jnp.sum(_stack_heads(dly[:, gs]), axis=-1, keepdims=True)
            sink = _per_head_rows([sk_ref[4 * g + j] for j in range(4)])
            pr = jnp.where(valid, jnp.exp(_dot(q4, kcat, NT) - lse4), 0.0)
            dsb = (pr * (_dot(dy4, vcat, NT) - delta)).astype(BF16)
            dsink = jnp.exp(sink - lse4) * delta
            for j in range(4):
                dsk_ref[4 * g + j:4 * g + j + 1, :] += jnp.broadcast_to(
                    -jnp.sum(dsink[j * BLK:(j + 1) * BLK], axis=0, keepdims=True), (1, 128))
            dqs.append(_unstack_heads(_dot(dsb, kcat)))
            dkg = _dot(dsb, q4, TN)
            dvg = _dot(pr.astype(BF16), dy4, TN)
            dkw_ref[0, :, gs] = dkg[:3 * BLK]
            dvw_ref[0, :, gs] = dvg[:3 * BLK]
            dkc_ref[:, gs] += dkg[3 * BLK:]
            dvc_ref[:, gs] += dvg[3 * BLK:]
        dq_ref[...] = jnp.concatenate(dqs, axis=1)

    blk, ctx = _attn_specs(nb)
    out = pl.BlockSpec((BLK, ATW), lambda i: (i, 0))
    win = pl.BlockSpec((1, 3 * BLK, ATW), lambda i: (i, 0, 0))
    acc = _full((L, ATW))
    return _pcall(body, name="attn_bwd", grid=(nb,),
                  in_specs=[pl.BlockSpec(memory_space=pltpu.SMEM), blk(0),
                            blk(-1), blk(0), blk(1), ctx, blk(-1), blk(0), blk(1), ctx, out, out, out],
                  out_specs=[out, win, win, acc, acc, _full((8, 128))],
                  out_shape=[jax.ShapeDtypeStruct((s_len, ATW), F32),
                             jax.ShapeDtypeStruct((nb, 3 * BLK, ATW), F32),
                             jax.ShapeDtypeStruct((nb, 3 * BLK, ATW), F32),
                             jax.ShapeDtypeStruct((L, ATW), F32), jax.ShapeDtypeStruct((L, ATW), F32),
                             jax.ShapeDtypeStruct((8, 128), F32)], carry=carry)(
                      sinks, qr, k4, k4, k4, k4, v4, v4, v4, v4, y, lse, dy)


def _attn_post(p, cos, sin, qnw8, knw2, bd512, bd128, dupt, dq, dkw, dvw, dkc, dvc, dp, carry=None):
    tt = p.shape[0]
    s_len = tt - L
    nb = s_len // BLK
    nctx = L // BLK

    def body(q_ref, kv_ref, cos_ref, sin_ref, qw_ref, kw_ref, b5_ref, b1_ref, dupt_ref,
             dq_ref, kwp, kwo, kwn, vwp, vwo, vwn, dkc_ref, dvc_ref, _dp_in,
             dp_ref, dqw_ref, dkw_ref):
        t = pl.program_id(0)
        j = t - nctx

        @pl.when(t == 0)
        def _():
            dqw_ref[...] = jnp.zeros_like(dqw_ref)
            dkw_ref[...] = jnp.zeros_like(dkw_ref)

        is_lat = t >= nctx
        cos_, sin_ = cos_ref[...], sin_ref[...]
        has_p = is_lat & (j >= 1)
        has_n = is_lat & (j <= nb - 2)
        dk4 = (jnp.where(is_lat, kwo[0], dkc_ref[...]) + jnp.where(has_p, kwp[0], 0.0)
               + jnp.where(has_n, kwn[0], 0.0))
        dv4 = (jnp.where(is_lat, vwo[0], dvc_ref[...]) + jnp.where(has_p, vwp[0], 0.0)
               + jnp.where(has_n, vwn[0], 0.0))
        dkr = _dot(dk4, dupt_ref[...], prec=HI)
        dv = _dot(dv4, dupt_ref[...], prec=HI)
        kv = kv_ref[...]
        k = kv[:, :128]
        kw = kw_ref[...]
        rk = lax.rsqrt(_head_mean(k * k, b1_ref[...]) + EPS)
        xk = k * rk
        dkn = dkr * cos_ + _rot(dkr * sin_)
        dxk = dkn * kw
        dk = rk * (dxk - xk * _head_mean(dxk * xk, b1_ref[...]))
        dkw_ref[...] += jnp.sum(dkn * xk, axis=0, keepdims=True)
        q = q_ref[...]
        qw = qw_ref[...]
        rq = lax.rsqrt(_head_mean(q * q, b5_ref[...]) + EPS)
        xq = q * rq
        cos4 = jnp.concatenate([cos_] * 4, axis=1)
        sin4 = jnp.concatenate([sin_] * 4, axis=1)
        dqr = jnp.where(is_lat, dq_ref[...], 0.0) * (HDIM ** -0.5)
        dqn = dqr * cos4 + _rot(dqr * sin4)
        dxq = dqn * qw
        dqraw = rq * (dxq - xq * _head_mean(dxq * xq, b5_ref[...]))
        dqw_ref[...] += jnp.sum(dqn * xq, axis=0, keepdims=True)
        dp_ref[...] = jnp.concatenate([dqraw, dk, dv], axis=1).astype(BF16)

    row = lambda w, cb: pl.BlockSpec((BLK, w), lambda t: (t, cb))
    lat = pl.BlockSpec((BLK, ATW), lambda t: (jnp.maximum(t - nctx, 0), 0))

    def part(off):
        return pl.BlockSpec((1, BLK, ATW), lambda t: (jnp.clip(t - nctx + off, 0, nb - 1), 1 - off, 0))

    cacc = pl.BlockSpec((BLK, ATW), lambda t: (jnp.minimum(t, nctx - 1), 0))
    return _pcall(body, name="attn_post", grid=(tt // BLK,),
                  in_specs=[row(ATW, C_QRAW), row(256, C_KV), row(128, 0), row(128, 0),
                            _full((1, ATW)), _full((1, 128)), _full((ATW, ATW)), _full((128, 128)),
                            _full((ATW, 128)), lat, part(-1), part(0), part(1), part(-1), part(0), part(1),
                            cacc, cacc, ANY],
                  out_specs=[pl.BlockSpec((BLK, 768), lambda t: (t, C_QKV)), _full((1, ATW)), _full((1, 128))],
                  out_shape=[jax.ShapeDtypeStruct(dp.shape, BF16), jax.ShapeDtypeStruct((1, ATW), F32),
                             jax.ShapeDtypeStruct((1, 128), F32)],
                  aliases={18: 0}, carry=carry)(p, p, cos, sin, qnw8, knw2, bd512, bd128, dupt,
                                   dq, dkw, dkw, dkw, dvw, dvw, dvw, dkc, dvc, dp)


def _branch_merge(y_hg, y_at, bh4, ba4, p):
    s_len = y_hg.shape[0]

    def body(yh_ref, ya_ref, bh_ref, ba_ref, gh_ref, ga_ref, ah_ref, aa_ref, m_ref):
        yh, ya = yh_ref[...], ya_ref[...]
        ah = jnp.concatenate([_bdot(yh, bh_ref[j]) for j in range(4)], axis=1)
        aa = jnp.concatenate([_bdot(ya, ba_ref[j]) for j in range(4)], axis=1)
        ah_ref[...] = ah.astype(BF16)
        aa_ref[...] = aa.astype(BF16)
        m_ref[...] = (_sig(gh_ref[...]) * ah + _sig(ga_ref[...]) * aa).astype(BF16)

    row = pl.BlockSpec((TM, D), lambda i: (i, 0))
    y = pl.BlockSpec((TM, HGW), lambda i: (i, 0))
    f = jax.ShapeDtypeStruct((s_len, D), BF16)
    return _pcall(body, name="branch_merge", grid=(s_len // TM,),
                  in_specs=[y, y, _full(bh4.shape), _full(ba4.shape),
                            pl.BlockSpec((TM, D), lambda i: (i + 1, 2)), pl.BlockSpec((TM, D), lambda i: (i + 1, 3))],
                  out_specs=[row, row, row],
                  out_shape=[f, f, jax.ShapeDtypeStruct((s_len, D), BF16)])(y_hg, y_at, bh4, ba4, p, p)


def _branch_bwd(dmh, dma, bh4, ba4, y_hg, y_at):
    s_len = dmh.shape[0]
    nk = s_len // TS
    ns = D // 4

    def body(dh_ref, da_ref, bh_ref, ba_ref, yh_ref, ya_ref, dyh_ref, dya_ref, gh_ref, ga_ref, acc_h, acc_a):
        t = pl.program_id(0)

        @pl.when(t == 0)
        def _():
            acc_h[...] = jnp.zeros_like(acc_h)
            acc_a[...] = jnp.zeros_like(acc_a)

        for d_ref, w_ref, y_ref, dy_ref, acc in ((dh_ref, bh_ref, yh_ref, dyh_ref, acc_h),
                                                 (da_ref, ba_ref, ya_ref, dya_ref, acc_a)):
            y = y_ref[...]
            dy = jnp.zeros((TS, HGW), F32)
            for j in range(4):
                dj = d_ref[:, j * ns:(j + 1) * ns]
                dy = dy + _bdot(dj, w_ref[j], NT)
                acc[j] += _bdot(y, dj, TN)
            dy_ref[...] = dy

        @pl.when(t == nk - 1)
        def _():
            gh_ref[...] = acc_h[...].astype(BF16)
            ga_ref[...] = acc_a[...].astype(BF16)

    dm = pl.BlockSpec((TS, D), lambda t: (t, 0))
    y = pl.BlockSpec((TS, HGW), lambda t: (t, 0))
    w = _full(bh4.shape)
    fy = jax.ShapeDtypeStruct((s_len, HGW), F32)
    gw = jax.ShapeDtypeStruct(bh4.shape, BF16)
    return _pcall(body, name="branch_bwd", grid=(nk,), in_specs=[dm, dm, w, w, y, y],
                  out_specs=[y, y, w, w], out_shape=[fy, fy, gw, gw],
                  scratch=[pltpu.VMEM(bh4.shape, F32)] * 2)(dmh, dma, bh4, ba4, y_hg, y_at)


def _merge_bwd(dattn, w_o, mixed, ah, aa, p, carry=None):
    tt = p.shape[0]
    s_len = tt - L
    nt = tt // TM

    def body(da_ref, wo_ref, mx_ref, ah_ref, aa_ref, gh_ref, ga_ref, dp_ref, dmh_ref, dma_ref, go_ref, acc):
        i = pl.program_id(0)

        @pl.when(i == 0)
        def _():
            dp_ref[...] = jnp.zeros_like(dp_ref)
            acc[...] = jnp.zeros_like(acc)

        @pl.when(i >= 1)
        def _():
            da = da_ref[...]
            acc[...] += _bdot(mx_ref[...], da, TN)
            dm_ = _bdot(da, wo_ref[...], NT)
            sh, sa = _sig(gh_ref[...]), _sig(ga_ref[...])
            dp_ref[...] = jnp.concatenate([dm_ * ah_ref[...].astype(F32) * sh * (1.0 - sh),
                                           dm_ * aa_ref[...].astype(F32) * sa * (1.0 - sa)], axis=1).astype(BF16)
            dmh_ref[...] = (dm_ * sh).astype(BF16)
            dma_ref[...] = (dm_ * sa).astype(BF16)

        @pl.when(i == nt - 1)
        def _():
            go_ref[...] = acc[...].astype(BF16)

    lat = pl.BlockSpec((TM, D), lambda i: (jnp.maximum(i - 1, 0), 0))
    return _pcall(body, name="merge_bwd", grid=(nt,),
                  in_specs=[lat, _full((D, D)), lat, lat, lat, pl.BlockSpec((TM, D), lambda i: (i, 2)),
                            pl.BlockSpec((TM, D), lambda i: (i, 3))],
                  out_specs=[pl.BlockSpec((TM, 2 * D), lambda i: (i, C_GATES)), lat, lat, _full((D, D))],
                  out_shape=[jax.ShapeDtypeStruct((tt, NCOL), BF16), jax.ShapeDtypeStruct((s_len, D), BF16),
                             jax.ShapeDtypeStruct((s_len, D), BF16), jax.ShapeDtypeStruct((D, D), BF16)],
                  scratch=[pltpu.VMEM((D, D), F32)], carry=carry)(dattn, w_o, mixed, ah, aa, p, p)


def _local_step(x, ctx, tgt, mod, modc, nw1, nw2, lg, hw, qnw, knw, sinks,
                w_in, wts, dist=None):
    s_len = x.shape[0]
    tt = s_len + L
    ss1 = jnp.stack([modc, mod[0:2]])
    ss2 = mod[3:5][None]
    g1, g2 = mod[2:3], mod[5:6]
    hw4 = jnp.tile(hw, (1, 4))
    qnw8 = jnp.tile(qnw, (1, 8))
    knw2 = jnp.tile(knw, (1, 2))
    cos, sin = _rope_tables(s_len)
    bd512, bd128 = _blockdiag(ATW, HDIM), _blockdiag(128, HDIM)
    dupm = _dup_matrix()
    dup, dupt = jnp.asarray(dupm, BF16), jnp.asarray(dupm.T, F32)
    tmt = tt

    def four(b):
        return b.reshape(4, 2 * b.shape[1], b.shape[2])

    def halves(g):
        return g.reshape(4, 2, g.shape[1] // 2, g.shape[2])

    h = _mod1(ctx, x, nw1, ss1)
    if dist is None:
        bh4, ba4, w_o, g4, u4, dn4 = wts
        p = _mm_in(h, w_in, tmt)
        o0, st0 = _hgrn_fwd(p, lg, rev=False)
        o1, st1, y_hg = _hgrn_fwd(p, lg, rev=True, readout=(o0, hw4))
    else:
        core, chip = dist
        half = wts[3].shape[1] // 2
        p, first = _mm_in(h, w_in, tmt, carry=_carry_join(_carry_gather(list(wts[0:3])),
                                                          _carry_gather([wts[3]], rows=[(0, half)])))
        (o0, st0), (g8,) = _hgrn_fwd(p, lg, rev=False, carry=_carry_gather([first[3]], rows=[(half, half)]))
        (o1, st1, y_hg), (dn8a,) = _hgrn_fwd(p, lg, rev=True, readout=(o0, hw4),
                                             carry=_carry_gather([wts[5]], rows=[(0, half)]))
        bh4, ba4, w_o, g4 = four(first[0]), four(first[1]), four(first[2]).reshape(D, D), four(g8)
    qr, k4, v4 = _qk_prep(p, cos, sin, qnw8, knw2, bd512, bd128, dup)
    if dist is None:
        y_at, lse = _attn_fwd(qr, k4, v4, sinks)
    else:
        (y_at, lse), (u8, dn8) = _attn_fwd(qr, k4, v4, sinks,
                                           carry=_carry_gather([wts[4], dn8a], rows=[None, (half, half)]))
        u4, dn4 = four(u8), four(dn8)
    ah, aa, mixed = _branch_merge(y_hg, y_at, bh4, ba4, p)
    ao, x1, h2 = _out_proj_mod2(mixed, w_o, x, g1, nw2, ss2)
    a4, b4, z4 = _ffn_up(h2, g4, u4)
    sq, dx2, dyb, dg2 = _ffn_down_loss(z4, dn4, x1, g2, tgt)

    da4, db4 = _ffn_dz(dyb, dn4, a4, b4)
    g_dn = _ffn_gdn(z4, dyb)
    if dist is None:
        dh2 = _ffn_dh2(da4, db4, g4, u4)
    else:
        dn_units = [halves(g_dn)]
        dh2, dn_recv = _ffn_dh2(da4, db4, g4, u4, carry=_carry_pairx(dn_units))
        dn_pairs = _rs_pair_add(dn_units, dn_recv, core)
    if dist is None:
        g_g, g_u = _ffn_ggu(h2, da4, db4)
    else:
        (g_g, g_u), c_dn = _ffn_ggu(h2, da4, db4, carry=_carry_chipx(dn_pairs))
        red_dn = _rs_chip_add(dn_pairs, c_dn, core, chip)
    dx1, dattn, dss2, dnw2, dg1 = _mod2_bwd(x1, dh2, dx2, ao, nw2, ss2, g1)
    if dist is None:
        dp, dmh, dma, g_o = _merge_bwd(dattn, w_o, mixed, ah, aa, p)
    else:
        gu_units = [halves(g_g), halves(g_u)]
        (dp, dmh, dma, g_o), gu_recv = _merge_bwd(dattn, w_o, mixed, ah, aa, p, carry=_carry_pairx(gu_units))
        ffn_pairs = list(dn_pairs) + list(_rs_pair_add(gu_units, gu_recv, core))
    dy_hg, dy_at, g_bh, g_ba = _branch_bwd(dmh, dma, bh4, ba4, y_hg, y_at)
    if dist is None:
        dp, do, dhw4 = _readout_bwd(o0, o1, p, hw4, dy_hg, dp)
        dq, dkw, dvw, dkc, dvc, dsk = _attn_bwd(qr, k4, v4, sinks, y_at, lse, dy_at)
        dp, dqnw8, dknw2 = _attn_post(p, cos, sin, qnw8, knw2, bd512, bd128, dupt, dq, dkw, dvw, dkc, dvc, dp)
    else:
        mix_units = [halves(g_bh), halves(g_ba), halves(g_o.reshape(4, D // 4, D))]
        (dp, do, dhw4), mix_recv = _readout_bwd(o0, o1, p, hw4, dy_hg, dp, carry=_carry_pairx(mix_units))
        mix_pairs = _rs_pair_add(mix_units, mix_recv, core)
        (dq, dkw, dvw, dkc, dvc, dsk), bwd = _attn_bwd(
            qr, k4, v4, sinks, y_at, lse, dy_at,
            carry=_carry_join(_carry_chipx(ffn_pairs[1:2]), _carry_sibx(red_dn)))
        red_g = _rs_chip_add(ffn_pairs[1:2], bwd[0:1], core, chip)
        (dp, dqnw8, dknw2), post = _attn_post(
            p, cos, sin, qnw8, knw2, bd512, bd128, dupt, dq, dkw, dvw, dkc, dvc, dp, carry=_carry_sibx(red_g))
    if dist is None:
        dp, dv0, dq0, dlg0 = _hgrn_bwd(p, lg, do, st0, dp, None, rev=False)
        dp, dlg1 = _hgrn_bwd(p, lg, do, st1, dp, (dv0, dq0), rev=True)
    else:
        (dp, dv0, dq0, dlg0), c_u = _hgrn_bwd(p, lg, do, st0, dp, None, rev=False,
                                              carry=_carry_chipx(ffn_pairs[2:3]))
        red_u = _rs_chip_add(ffn_pairs[2:3], c_u, core, chip)
        (dp, dlg1), last = _hgrn_bwd(p, lg, do, st1, dp, (dv0, dq0), rev=True,
                                     carry=_carry_join(_carry_chipx(mix_pairs), _carry_sibx(red_u)))
        mix_reds = _rs_chip_add(mix_pairs, last[0:3], core, chip)
        ffn_done = bwd[1:2] + post[0:1] + last[3:4]
    g_in = _mm_gin(dp, h, tmt)
    if dist is None:
        dh = _mm_dh(dp, w_in, tmt)
        gx, dss1, dnw1 = _mod1_bwd(ctx, x, dh, dx1, nw1, ss1)
        rs = None
    else:
        in_units = [halves(g_in.reshape(4, NCOL // 4, D))]
        in_pairs = _rs_pair_add(in_units, _pair_exchange(in_units), core)
        first_rows, rest_rows = _split_rows(in_pairs[0].shape[1], IN_ROWS_WITH_DH)
        dh, both = _mm_dh(dp, w_in, tmt, carry=_carry_join(_carry_chipx(in_pairs, rows=first_rows),
                                                           _carry_sibx(mix_reds)))
        in_part, mix_done = both[0:1], both[1:4]
        gx, dss1, dnw1 = _mod1_bwd(ctx, x, dh, dx1, nw1, ss1)
        rs = dict(ffn_done=ffn_done, mix_done=mix_done, in_pairs=in_pairs, in_part=in_part, rest_rows=rest_rows)

    dmod = jnp.concatenate([dss1[1], dg1, dss2, dg2], axis=0)
    dmodc = dss1[0]
    raw = (dss1, dg1, dss2, dg2, dnw1, dnw2, dhw4, dqnw8, dknw2, dsk, dlg0, dlg1)
    small = dict(raw=raw, dmod=dmod, dmodc=dmodc, dnw1=dnw1, dnw2=dnw2,
                 dhw=dhw4.reshape(4, HGD).sum(0, keepdims=True),
                 dqnw=dqnw8.reshape(8, HDIM).sum(0, keepdims=True),
                 dknw=dknw2.reshape(2, HDIM).sum(0, keepdims=True),
                 dsinks=dsk[:, 0], dlg=jnp.concatenate([dlg0, dlg1], axis=0))
    big = dict(w_in=g_in, w_bh=g_bh, w_ba=g_ba, w_o=g_o, w_g=g_g, w_u=g_u, w_dn=g_dn)
    return sq, gx, big, small, rs


def _place():
    x, y, c = lax.axis_index("x"), lax.axis_index("y"), lax.axis_index("c")
    return x, y, c


def _gather_blocks(x_refs, out_refs, send_sems, recv_sems, local_sems):
    n = len(out_refs)
    x, y, c = _place()
    me, sibling = (x, y, c), (x, y, 1 - c)
    chips = [(1 - x, y), (x, 1 - y), (1 - x, 1 - y)]

    def slot(u, px, py, pc):
        return out_refs[u].at[4 * px + 2 * py + pc]

    def copy(u, k, block, to, src=None):
        return pltpu.make_async_remote_copy(
            src_ref=slot(u, *block) if src is None else src, dst_ref=slot(u, *block),
            send_sem=send_sems.at[u, k], recv_sem=recv_sems.at[u, k], device_id=to, device_id_type=MESH)

    mines = [pltpu.make_async_copy(x_refs[u], slot(u, *me), local_sems.at[u]) for u in range(n)]
    for cp in mines:
        cp.start()
    first = []
    for u in range(n):
        first.append(copy(u, 0, me, sibling, src=x_refs[u]))
        first += [copy(u, 1 + j, me, (*chip, c), src=x_refs[u]) for j, chip in enumerate(chips)]
    for cp in first:
        cp.start()
    passed = []
    for j, chip in enumerate(chips):
        for u in range(n):
            copy(u, 1 + j, (*chip, c), me).wait_recv()
            fwd = copy(u, 4 + j, (*chip, c), sibling)
            fwd.start()
            passed.append(fwd)
    for u in range(n):
        copy(u, 0, sibling, me).wait_recv()
    for j, chip in enumerate(chips):
        for u in range(n):
            copy(u, 4 + j, (*chip, 1 - c), me).wait_recv()
    for cp in first + passed:
        cp.wait_send()
    for cp in mines:
        cp.wait()


def _gather_sems(n):
    return [pltpu.SemaphoreType.DMA((n, 7)), pltpu.SemaphoreType.DMA((n, 7)), pltpu.SemaphoreType.DMA((n,))]


def _cast_place(ws, c, dev):
    n = len(ws)

    def body(s_ref, *refs):
        for u in range(n):
            refs[n + u][0] = refs[u][...].astype(BF16)

    in_specs, out_specs, out_shape = [], [], []
    for w in ws:
        q, cols = w.shape[0] // 4, w.shape[1]
        in_specs.append(pl.BlockSpec((q, cols), lambda i, s: (2 * s[0] + i, 0)))
        out_specs.append(pl.BlockSpec((1, q, cols), lambda i, s: (s[1], i, 0)))
        out_shape.append(jax.ShapeDtypeStruct((8, 2 * q, cols), BF16))
    return pl.pallas_call(
        body, name="cast_place",
        grid_spec=pltpu.PrefetchScalarGridSpec(num_scalar_prefetch=1, grid=(2,), in_specs=in_specs,
                                               out_specs=out_specs),
        out_shape=_out_hbm(out_shape),
        compiler_params=pltpu.CompilerParams(vmem_limit_bytes=48 << 20))(jnp.stack([c, dev]), *_in_hbm(ws))


def _gather_phases(out_refs, send_sems, recv_sems, rows=None):
    n = len(out_refs)
    x, y, c = _place()
    me, sibling = (x, y, c), (x, y, 1 - c)
    chips = [(1 - x, y), (x, 1 - y), (1 - x, 1 - y)]

    def copy(u, k, block, to):
        px, py, pc = block
        ref = out_refs[u].at[4 * px + 2 * py + pc]
        if rows is not None and rows[u] is not None:
            ref = ref.at[pl.ds(rows[u][0], rows[u][1])]
        return pltpu.make_async_remote_copy(src_ref=ref, dst_ref=ref, send_sem=send_sems.at[u, k],
                                            recv_sem=recv_sems.at[u, k], device_id=to, device_id_type=MESH)

    def start():
        for u in range(n):
            copy(u, 0, me, sibling).start()
            for j, chip in enumerate(chips):
                copy(u, 1 + j, me, (*chip, c)).start()

    def mid():
        for j, chip in enumerate(chips):
            for u in range(n):
                copy(u, 1 + j, (*chip, c), me).wait_recv()
                copy(u, 4 + j, (*chip, c), sibling).start()

    def end():
        for u in range(n):
            copy(u, 0, sibling, me).wait_recv()
        for j, chip in enumerate(chips):
            for u in range(n):
                copy(u, 4 + j, (*chip, 1 - c), me).wait_recv()
        for u in range(n):
            copy(u, 0, me, sibling).wait_send()
            for j, chip in enumerate(chips):
                copy(u, 1 + j, me, (*chip, c)).wait_send()
                copy(u, 4 + j, (*chip, c), sibling).wait_send()

    return start, mid, end


def _carry_gather(bufs, rows=None):
    n = len(bufs)
    return _Carry(bufs, [jax.ShapeDtypeStruct(b.shape, b.dtype) for b in bufs], {u: u for u in range(n)},
                  [pltpu.SemaphoreType.DMA((n, 7)), pltpu.SemaphoreType.DMA((n, 7))],
                  lambda ins, outs, sems: _gather_phases(outs, *sems, rows=rows), "both")


def _ag_small(raw, sq):
    def body(dss1, dg1, dss2, dg2, dnw1, dnw2, dhw4, dqnw8, dknw2, dsk, dlg0, dlg1, sq_ref,
             out_ref, tot_ref, blk, send_sems, recv_sems, local_sems):
        _peer_barrier("both")
        blk[...] = jnp.zeros_like(blk)
        blk[0:2, :] = dss1[1]
        blk[2:3, :] = dg1[...]
        blk[3:5, :] = dss2[...]
        blk[5:6, :] = dg2[...]
        blk[6:8, :] = dss1[0]
        blk[8:9, :] = dnw1[...]
        blk[9:10, :] = dnw2[...]
        blk[10:11, 0:HGW] = dhw4[...]
        blk[10:11, HGW:D] = dqnw8[...]
        blk[11:12, 0:128] = dknw2[...]
        blk[12:14, 0:HGW] = dlg0[0]
        blk[14:16, 0:HGW] = dlg1[0]
        blk[16:24, 0:128] = dsk[...]
        blk[24:25, :] = sq_ref[...]
        _gather_blocks([blk], [out_ref], send_sems, recv_sems, local_sems)
        acc = out_ref[0]
        for i in range(1, 8):
            acc = acc + out_ref[i]
        tot_ref[...] = acc

    vm = pl.BlockSpec(memory_space=pltpu.VMEM)
    return pl.pallas_call(
        body, name="ag_small",
        out_shape=[jax.ShapeDtypeStruct((8, 32, D), F32), jax.ShapeDtypeStruct((32, D), F32)],
        in_specs=[vm] * 13, out_specs=[vm, vm],
        scratch_shapes=[pltpu.VMEM((32, D), F32)] + _gather_sems(1),
        compiler_params=pltpu.CompilerParams(collective_id=BARRIER_IDS["both"]))(*raw, sq)


def _pairx_shapes(units):
    return [jax.ShapeDtypeStruct((4,) + g.shape[2:], g.dtype) for g in units]


def _pairx_phases(g_refs, r_refs, send_sems, recv_sems):
    n = len(g_refs)
    x, y, c = _place()
    cps = [pltpu.make_async_remote_copy(
        src_ref=g_refs[u].at[j, 1 - c], dst_ref=r_refs[u].at[j], send_sem=send_sems.at[u, j],
        recv_sem=recv_sems.at[u, j], device_id=(x, y, 1 - c), device_id_type=MESH)
        for u in range(n) for j in range(4)]

    def start():
        for cp in cps:
            cp.start()

    def end():
        for cp in cps:
            cp.wait()

    return start, None, end


def _carry_pairx(units):
    n = len(units)
    return _Carry(units, _pairx_shapes(units), {},
                  [pltpu.SemaphoreType.DMA((n, 4)), pltpu.SemaphoreType.DMA((n, 4))],
                  lambda ins, outs, sems: _pairx_phases(ins, outs, *sems), "sib")


def _pair_exchange(units):
    n = len(units)

    def body(*refs):
        _peer_barrier("sib")
        start, _, end = _pairx_phases(refs[:n], refs[n:2 * n], *refs[2 * n:])
        start()
        end()

    return pl.pallas_call(
        body, name="pair_exchange", out_shape=_pairx_shapes(units), in_specs=[ANY] * n, out_specs=[ANY] * n,
        scratch_shapes=[pltpu.SemaphoreType.DMA((n, 4))] * 2,
        compiler_params=pltpu.CompilerParams(collective_id=BARRIER_IDS["sib"]))(*_in_hbm(units))


IN_ROWS_WITH_DH = 0.6


def _split_rows(h, share):
    first = int(h * share) // BF16_SUBLANES * BF16_SUBLANES
    return (0, first), (first, h - first)


def _rs_pair_add(units, recvs, c):
    n = len(units)

    def body(c_ref, *refs):
        for u in range(n):
            refs[2 * n + u][...] = (refs[u][0].astype(F32) + refs[n + u][...].astype(F32)).astype(BF16)

    in_specs, out_specs, out_shape = [], [], []
    for g in units:
        h, w = g.shape[2:]
        in_specs.append(pl.BlockSpec((1, 1, h, w), lambda j, cr: (j, cr[0], 0, 0)))
    for g in units:
        h, w = g.shape[2:]
        in_specs.append(pl.BlockSpec((1, h, w), lambda j, cr: (j, 0, 0)))
        out_specs.append(pl.BlockSpec((1, h, w), lambda j, cr: (j, 0, 0)))
        out_shape.append(jax.ShapeDtypeStruct((4, h, w), BF16))
    return pl.pallas_call(
        body, name="rs_pair_add",
        grid_spec=pltpu.PrefetchScalarGridSpec(num_scalar_prefetch=1, grid=(4,), in_specs=in_specs,
                                               out_specs=out_specs),
        out_shape=_out_hbm(out_shape),
        compiler_params=pltpu.CompilerParams(vmem_limit_bytes=48 << 20))(
            c.reshape(1), *_in_hbm(list(units) + list(recvs)))


def _chipx_phases(p_refs, r_refs, send_sems, recv_sems, rows=None):
    n = len(p_refs)
    x, y, c = _place()
    k = 2 * x + y

    def part(ref):
        return ref if rows is None else ref.at[pl.ds(rows[0], rows[1])]

    sends = []
    for d in range(1, 4):
        j = (k + d) % 4
        for u in range(n):
            sends.append(pltpu.make_async_remote_copy(
                src_ref=part(p_refs[u].at[j]), dst_ref=part(r_refs[u].at[k]), send_sem=send_sems.at[u, d - 1],
                recv_sem=recv_sems.at[u, d - 1], device_id=(j // 2, j % 2, c), device_id_type=MESH))

    def start():
        for cp in sends:
            cp.start()

    def end():
        for d in range(1, 4):
            src = (k + 4 - d) % 4
            for u in range(n):
                pltpu.make_async_remote_copy(
                    src_ref=part(p_refs[u].at[src]), dst_ref=part(r_refs[u].at[src]),
                    send_sem=send_sems.at[u, d - 1], recv_sem=recv_sems.at[u, d - 1], device_id=(x, y, c),
                    device_id_type=MESH).wait_recv()
        for cp in sends:
            cp.wait_send()

    return start, None, end


def _carry_chipx(pairs, rows=None, into=None):
    n = len(pairs)
    sems = [pltpu.SemaphoreType.DMA((n, 3)), pltpu.SemaphoreType.DMA((n, 3))]
    shapes = [jax.ShapeDtypeStruct(p.shape, p.dtype) for p in pairs]
    if into is None:
        return _Carry(pairs, shapes, {}, sems, lambda ins, outs, s: _chipx_phases(ins, outs, *s, rows=rows),
                      "chips")
    return _Carry(list(pairs) + list(into), shapes, {n + u: u for u in range(n)}, sems,
                  lambda ins, outs, s: _chipx_phases(ins[:n], outs, *s, rows=rows), "chips")


def _rs_chip_add(pairs, contribs, c, chip):
    n = len(pairs)

    def body(s_ref, *refs):
        for u in range(n):
            a, b, c_, d = refs[4 * u:4 * u + 4]
            refs[4 * n + u][0] = ((a[0].astype(F32) + b[0].astype(F32)) + c_[0].astype(F32)) + d[0].astype(F32)

    in_specs, out_specs, out_shape, args = [], [], [], []
    for p, r in zip(pairs, contribs):
        h, w = p.shape[1] // 2, p.shape[2]
        in_specs += [pl.BlockSpec((1, h, w), functools.partial(lambda d, i, s: ((s[1] + d) % 4, i, 0), d))
                     for d in range(4)]
        args += [p, r, r, r]
        out_specs.append(pl.BlockSpec((1, h, w), lambda i, s: (s[0], i, 0)))
        out_shape.append(jax.ShapeDtypeStruct((2, 2 * h, w), F32))
    return pl.pallas_call(
        body, name="rs_chip_add",
        grid_spec=pltpu.PrefetchScalarGridSpec(num_scalar_prefetch=1, grid=(2,), in_specs=in_specs,
                                               out_specs=out_specs),
        out_shape=_out_hbm(out_shape),
        compiler_params=pltpu.CompilerParams(vmem_limit_bytes=48 << 20))(jnp.stack([c, chip]), *_in_hbm(args))


def _rs_sibling_gather(reds, blks):
    n, k = len(reds), len(blks)
    vm = pl.BlockSpec(memory_space=pltpu.VMEM)

    def body(*refs):
        blk_in, red_out, blk_out = refs[n:n + k], refs[n + k:2 * n + k], refs[2 * n + k:2 * (n + k)]
        sems = refs[2 * (n + k):]
        _peer_barrier("both")
        start, _, end = _sibx_phases(red_out, *sems[:2])
        start()
        _gather_blocks(blk_in, blk_out, *sems[2:])
        end()

    res = pl.pallas_call(
        body, name="rs_sibling_gather",
        out_shape=[jax.ShapeDtypeStruct(r.shape, r.dtype) for r in reds]
        + [jax.ShapeDtypeStruct((8,) + b.shape, b.dtype) for b in blks],
        in_specs=[ANY] * n + [vm] * k, out_specs=[ANY] * n + [vm] * k,
        input_output_aliases={u: u for u in range(n)},
        scratch_shapes=[pltpu.SemaphoreType.DMA((n,))] * 2 + _gather_sems(k),
        compiler_params=pltpu.CompilerParams(collective_id=BARRIER_IDS["both"]))(*_in_hbm(reds), *blks)
    return res[:n], res[n:]


def _sibx_phases(o_refs, send_sems, recv_sems):
    n = len(o_refs)
    x, y, c = _place()
    cps = [pltpu.make_async_remote_copy(
        src_ref=o_refs[u].at[c], dst_ref=o_refs[u].at[c], send_sem=send_sems.at[u], recv_sem=recv_sems.at[u],
        device_id=(x, y, 1 - c), device_id_type=MESH) for u in range(n)]

    def start():
        for cp in cps:
            cp.start()

    def end():
        for u in range(n):
            cps[u].wait_send()
            pltpu.make_async_remote_copy(
                src_ref=o_refs[u].at[1 - c], dst_ref=o_refs[u].at[1 - c], send_sem=send_sems.at[u],
                recv_sem=recv_sems.at[u], device_id=(x, y, 1 - c), device_id_type=MESH).wait_recv()

    return start, None, end


def _carry_sibx(reds):
    n = len(reds)
    return _Carry(reds, [jax.ShapeDtypeStruct(r.shape, r.dtype) for r in reds], {u: u for u in range(n)},
                  [pltpu.SemaphoreType.DMA((n,))] * 2, lambda ins, outs, sems: _sibx_phases(outs, *sems), "sib")


def _prologue(blk, c_ctx, w, b, in8):
    n = w.shape[1]

    def body(blk_ref, cctx_ref, w_ref, b_ref, _in_in, g0_ref, c16_ref, g1_ref, in_ref, s1, r1, l1, s2, r2, s3, r3):
        _peer_barrier("both")
        x, y, c = _place()
        start, mid, end = _gather_phases([in_ref], s3, r3)
        start_mod, mid_mod, end_mod = _gather_phases([g1_ref], s2, r2)
        _gather_blocks([blk_ref], [g0_ref], s1, r1, l1)
        start()
        c16 = jnp.concatenate([g0_ref[i, 0:1, :] for i in range(8)] + [cctx_ref[...], jnp.zeros((7, D), F32)],
                              axis=0)
        c16_ref[...] = c16
        g1_ref[4 * x + 2 * y + c] = _dot(c16 * _sig(c16), w_ref[...], prec=HI) + b_ref[...]
        start_mod()
        mid()
        mid_mod()
        end()
        end_mod()

    vm = pl.BlockSpec(memory_space=pltpu.VMEM)
    return pl.pallas_call(
        body, name="prologue",
        out_shape=[jax.ShapeDtypeStruct((8, 8, D), F32), jax.ShapeDtypeStruct((16, D), F32),
                   jax.ShapeDtypeStruct((8, 16, n), F32), jax.ShapeDtypeStruct(in8.shape, in8.dtype)],
        in_specs=[vm, vm, vm, vm, ANY], out_specs=[vm, vm, vm, ANY], input_output_aliases={4: 3},
        scratch_shapes=_gather_sems(1) + [pltpu.SemaphoreType.DMA((1, 7))] * 4,
        compiler_params=pltpu.CompilerParams(vmem_limit_bytes=48 << 20,
                                             collective_id=BARRIER_IDS["both"]))(blk, c_ctx, w, b, in8)


def _ada_bwd(c16, dmod16, w, carry=None):
    n = w.shape[1]
    tn = 512

    def body(c_ref, d_ref, w_ref, gw_ref, gc_ref):
        j = pl.program_id(0)

        @pl.when(j == 0)
        def _():
            gc_ref[...] = jnp.zeros_like(gc_ref)

        cc = c_ref[...]
        dm = d_ref[...]
        gw_ref[...] = _dot(cc * _sig(cc), dm, TN, prec=HI)
        gc_ref[...] += _dot(dm, w_ref[...], NT, prec=HI)

    return _pcall(body, name="ada_bwd", grid=(n // tn,),
                  in_specs=[_full((16, D)), pl.BlockSpec((16, tn), lambda j: (0, j)),
                            pl.BlockSpec((D, tn), lambda j: (0, j))],
                  out_specs=[pl.BlockSpec((D, tn), lambda j: (0, j)), _full((16, D))],
                  out_shape=[jax.ShapeDtypeStruct((D, n), F32),
                             jax.ShapeDtypeStruct((16, D), F32)], carry=carry)(c16, dmod16, w)


def _adam_math(w, g, m, v):
    c1 = 1.0 - ADAM_B1 ** ADAM_STEP
    c2 = 1.0 - ADAM_B2 ** ADAM_STEP
    nm = ADAM_B1 * m + (1.0 - ADAM_B1) * g
    nv = ADAM_B2 * v + (1.0 - ADAM_B2) * (g * g)
    return -ADAM_LR * ((nm / c1) / (jnp.sqrt(nv / c2) + ADAM_EPS) + ADAM_WD * w), nm, nv


def _adamw_small(ws, gs, ms, vs):
    n = len(ws)

    def body(*refs):
        for u in range(n):
            d_, nm, nv = _adam_math(refs[u][...], refs[n + u][...], refs[2 * n + u][...], refs[3 * n + u][...])
            refs[4 * n + u][...] = d_
            refs[5 * n + u][...] = nm
            refs[6 * n + u][...] = nv

    specs = [_full(w.shape) for w in ws]
    shapes = [jax.ShapeDtypeStruct(w.shape, F32) for w in ws]
    out = _pcall(body, name="adamw_small", grid=(1,), in_specs=specs * 4, out_specs=specs * 3,
                 out_shape=shapes * 3)(*ws, *gs, *ms, *vs)
    return out[:n], out[n:2 * n], out[2 * n:]


def _cctx_grad(parts, c_ctx):
    def body(p_ref, c_ref, o_ref):
        acc = p_ref[0:1, :]
        for k in range(1, 4):
            acc = acc + p_ref[k:k + 1, :]
        cc = c_ref[...]
        s = _sig(cc)
        o_ref[...] = acc * (s * (1.0 + cc * (1.0 - s)))

    return _pcall(body, name="cctx_grad", grid=(1,), in_specs=[_full(parts.shape), _full((1, D))],
                  out_specs=_full((1, D)), out_shape=jax.ShapeDtypeStruct((1, D), F32))(parts, c_ctx)


ADAM_STEPS = 8


def _adamw_multi(ws, gs, ms, vs, *, name):
    n = len(ws)

    def body(*refs):
        for u in range(n):
            g = refs[n + u][...]
            refs[4 * n + u][...] = g
            refs[5 * n + u][...], refs[6 * n + u][...], refs[7 * n + u][...] = _adam_math(
                refs[u][...], g, refs[2 * n + u][...], refs[3 * n + u][...])

    specs = [pl.BlockSpec((w.shape[0] // ADAM_STEPS, w.shape[1]), lambda i: (i, 0)) for w in ws]
    shapes = [jax.ShapeDtypeStruct(w.shape, F32) for w in ws]
    out = _pcall(body, name=name, grid=(ADAM_STEPS,), in_specs=specs * 4, out_specs=specs * 4,
                 out_shape=shapes * 4)(*ws, *gs, *ms, *vs)
    return out[:n], out[n:2 * n], out[2 * n:3 * n], out[3 * n:]


def kernel(x, c, ctx, c_ctx, w_ada, b_ada, norm_mix_w, norm_ffn_w, w_in, hgrn_lb_logits, hgrn_norm_w, q_norm_w, k_norm_w, attn_sinks, w_branch_hgrn, w_branch_attn, w_out, w_ffn_gate, w_ffn_up, w_ffn_down, loss_target, m_c_ctx, m_w_ada, m_b_ada, m_norm_mix_w, m_norm_ffn_w, m_w_in, m_hgrn_lb_logits, m_hgrn_norm_w, m_q_norm_w, m_k_norm_w, m_attn_sinks, m_w_branch_hgrn, m_w_branch_attn, m_w_out, m_w_ffn_gate, m_w_ffn_up, m_w_ffn_down, v_c_ctx, v_w_ada, v_b_ada, v_norm_mix_w, v_norm_ffn_w, v_w_in, v_hgrn_lb_logits, v_hgrn_norm_w, v_q_norm_w, v_k_norm_w, v_attn_sinks, v_w_branch_hgrn, v_w_branch_attn, v_w_out, v_w_ffn_gate, v_w_ffn_up, v_w_ffn_down):
    xi, yi, ci = _place()
    chip = 2 * xi + yi
    dev = 2 * chip + ci
    s_len = x.shape[1]

    shards = [w_in[0].T, w_branch_hgrn[0], w_branch_attn[0], w_out[0], w_ffn_gate[0].T, w_ffn_up[0].T,
              w_ffn_down[0]]
    bufs = _cast_place(shards, ci, dev)

    lbrow = jnp.pad(hgrn_lb_logits.reshape(1, 512), ((0, 0), (0, D - 512)))
    blk = jnp.concatenate([c, lbrow, jnp.zeros((6, D), F32)], axis=0)
    nada = w_ada.shape[2]
    b_sh = lax.dynamic_slice(b_ada, (0, chip * nada), (1, nada))
    g0, c16, g1, in8 = _prologue(blk, c_ctx[None], w_ada[0], b_sh, bufs[0])
    lg = g0[0::2, 1, :512].reshape(4, 2, 2, 128).transpose(1, 2, 0, 3).reshape(2, 2, HGW)
    modall = g1[0::2].transpose(1, 0, 2).reshape(16, 4 * nada)
    mod = lax.dynamic_slice(modall, (dev, 0), (1, 6 * D)).reshape(6, D)
    modc = modall[8].reshape(6, D)[:2]

    sq, gx, _, small, rs = _local_step(
        x[0], ctx[0], loss_target[0], mod, modc, norm_mix_w, norm_ffn_w, lg, hgrn_norm_w, q_norm_w,
        k_norm_w, attn_sinks[0], in8.reshape(NCOL, D), bufs[1:], dist=(ci, chip))

    def whole(r):
        return r.reshape(2 * r.shape[1], r.shape[2])

    g_dn, g_g, g_u = [whole(r) for r in rs["ffn_done"]]
    g_bh, g_ba, g_o = [whole(r) for r in rs["mix_done"]]
    in_pairs = rs["in_pairs"]

    g2, tot = _ag_small(small["raw"], sq)
    loss = 0.5 * jnp.sum(tot[24]) / D
    dmodc_tot = jnp.pad(tot[6:8].reshape(1, 2 * D), ((0, 0), (0, 4 * D)))
    g_b_ada = tot[0:6].reshape(1, 6 * D) + dmodc_tot
    dmod16 = jnp.concatenate([g2[:, 0:6].reshape(8, 6 * D), dmodc_tot, jnp.zeros((7, 6 * D), F32)], axis=0)
    (g_w_ada, gc_part), in_contribs = _ada_bwd(
        c16, lax.dynamic_slice(dmod16, (0, chip * nada), (16, nada)), w_ada[0],
        carry=_carry_chipx(in_pairs, rows=rs["rest_rows"], into=rs["in_part"]))
    in_reds, (g3,) = _rs_sibling_gather(_rs_chip_add(in_pairs, in_contribs, ci, chip), [gc_part[8:16]])
    g_in = whole(in_reds[0])
    g_c_ctx = _cctx_grad(g3[0::2, 0], c_ctx[None])[0]
    g_nw1 = tot[8:9]
    g_nw2 = tot[9:10]
    g_hw = tot[10, :HGW].reshape(4, HGD).sum(0, keepdims=True)
    g_qnw = tot[10, HGW:].reshape(8, HDIM).sum(0, keepdims=True)
    g_knw = tot[11, :128].reshape(2, HDIM).sum(0, keepdims=True)
    g_sinks = tot[16:24, 0][None]
    g_lg = lax.dynamic_slice(tot[12:16, :HGW].reshape(2, 2, HGW), (0, 0, chip * 128), (2, 2, 128))

    names = ["c_ctx", "w_ada", "b_ada", "norm_mix_w", "norm_ffn_w", "w_in", "hgrn_lb_logits", "hgrn_norm_w",
             "q_norm_w", "k_norm_w", "attn_sinks", "w_branch_hgrn", "w_branch_attn", "w_out", "w_ffn_gate",
             "w_ffn_up", "w_ffn_down"]
    ws = dict(zip(names, [c_ctx, w_ada, b_ada, norm_mix_w, norm_ffn_w, w_in, hgrn_lb_logits, hgrn_norm_w,
                          q_norm_w, k_norm_w, attn_sinks, w_branch_hgrn, w_branch_attn, w_out, w_ffn_gate,
                          w_ffn_up, w_ffn_down]))
    ms = dict(zip(names, [m_c_ctx, m_w_ada, m_b_ada, m_norm_mix_w, m_norm_ffn_w, m_w_in, m_hgrn_lb_logits,
                          m_hgrn_norm_w, m_q_norm_w, m_k_norm_w, m_attn_sinks, m_w_branch_hgrn,
                          m_w_branch_attn, m_w_out, m_w_ffn_gate, m_w_ffn_up, m_w_ffn_down]))
    vs = dict(zip(names, [v_c_ctx, v_w_ada, v_b_ada, v_norm_mix_w, v_norm_ffn_w, v_w_in, v_hgrn_lb_logits,
                          v_hgrn_norm_w, v_q_norm_w, v_k_norm_w, v_attn_sinks, v_w_branch_hgrn,
                          v_w_branch_attn, v_w_out, v_w_ffn_gate, v_w_ffn_up, v_w_ffn_down]))
    transposed = ("w_in", "w_ffn_gate", "w_ffn_up")

    def view(a, n):
        return a[0].T if n in transposed else a[0]

    def unview(a, n):
        return a.T[None] if n in transposed else a[None]

    delta, new_m, new_v, grads = {}, {}, {}, {}

    def big_adamw(group, gs, name):
        g_, d_, m_, v_ = _adamw_multi([view(ws[n], n) for n in group], gs, [view(ms[n], n) for n in group],
                                      [view(vs[n], n) for n in group], name=name)
        for i, n in enumerate(group):
            grads[n], delta[n], new_m[n], new_v[n] = (unview(g_[i], n), unview(d_[i], n), unview(m_[i], n),
                                                      unview(v_[i], n))

    big_adamw(["w_ffn_down", "w_ffn_gate", "w_ffn_up", "w_out", "w_branch_hgrn", "w_branch_attn"],
              [g_dn, g_g, g_u, g_o, g_bh, g_ba], "adamw_first")
    big_adamw(["w_in", "w_ada"], [g_in, g_w_ada], "adamw_second")
    grads.update(c_ctx=g_c_ctx, b_ada=g_b_ada, norm_mix_w=g_nw1, norm_ffn_w=g_nw2, hgrn_lb_logits=g_lg,
                 hgrn_norm_w=g_hw, q_norm_w=g_qnw, k_norm_w=g_knw, attn_sinks=g_sinks)
    small_names = [n for n in names if n not in delta]

    def two_d(a):
        return a.reshape(1, -1) if a.ndim == 1 else a

    sd, sm_, sv = _adamw_small(*[[two_d(d[n]) for n in small_names] for d in (ws, grads, ms, vs)])
    for i, n in enumerate(small_names):
        for dst, src in ((delta, sd), (new_m, sm_), (new_v, sv)):
            dst[n] = src[i].reshape(ws[n].shape)
    return (loss, gx[None], *[grads[n] for n in names], *[delta[n] for n in names],
            *[new_m[n] for n in names], *[new_v[n] for n in names])
```

```python
import functools

import numpy as np
import jax
import jax.numpy as jnp
from jax import lax
from jax.experimental import pallas as pl
from jax.experimental.pallas import tpu as pltpu

F32 = jnp.float32
BF16 = jnp.bfloat16
HI = lax.Precision.HIGHEST
MESH = pl.DeviceIdType.MESH

D = 1024
L = 256
TM = 256
HGW = 512
HGD = 128
CH = 32
ATW = 512
HDIM = 64
BLK = 128
GRID_W = 64
DFF = 2816
NCOL = 5376
EPS = 1e-6
ROPE_THETA = 10000.0
BF16_SUBLANES = 16

C_FB, C_INP, C_QHG, C_FF = 0, 1, 2, 3
C_GATES = 1
C_GHG, C_QRAW = 8, 9
C_KV = 20
C_QKV = 6

ADAM_LR, ADAM_B1, ADAM_B2, ADAM_EPS, ADAM_WD, ADAM_STEP = 0.001, 0.9, 0.999, 1e-08, 0.01, 10

NN = (((1,), (0,)), ((), ()))
NT = (((1,), (1,)), ((), ()))
TN = (((0,), (0,)), ((), ()))


def _dot(a, b, dims=NN, prec=None):
    return lax.dot_general(a, b, dims, precision=prec, preferred_element_type=F32)


def _bdot(a, b, dims=NN):
    return _dot(a.astype(BF16), b.astype(BF16), dims)


def _sig(x):
    return 1.0 / (1.0 + jnp.exp(-x))


class _Carry:
    def __init__(self, ins, outs, aliases, scratch, phases, peers):
        self.ins, self.outs, self.aliases, self.scratch, self.phases = ins, outs, aliases, scratch, phases
        self.peers = peers


BARRIER_IDS = {"sib": 1, "chips": 2, "both": 3}


def _peer_barrier(kind):
    x, y, c = _place()
    peers = []
    if kind in ("sib", "both"):
        peers.append((x, y, 1 - c))
    if kind in ("chips", "both"):
        peers += [(1 - x, y, c), (x, 1 - y, c), (1 - x, 1 - y, c)]
    bar = pltpu.get_barrier_semaphore()
    for peer in peers:
        pl.semaphore_signal(bar, inc=1, device_id=peer, device_id_type=MESH)
    pl.semaphore_wait(bar, len(peers))


def _in_hbm(args):
    return [pltpu.with_memory_space_constraint(a, pltpu.HBM) for a in args]


def _out_hbm(shapes):
    if isinstance(shapes, (list, tuple)):
        return [pltpu.HBM(s.shape, s.dtype) for s in shapes]
    return pltpu.HBM(shapes.shape, shapes.dtype)


def _carry_join(a, b):
    na_in, na_out, na_sc = len(a.ins), len(a.outs), len(a.scratch)
    aliases = dict(a.aliases)
    aliases.update({na_in + i: na_out + o for i, o in b.aliases.items()})

    def phases(ins, outs, sems):
        pa = a.phases(ins[:na_in], outs[:na_out], sems[:na_sc])
        pb = b.phases(ins[na_in:], outs[na_out:], sems[na_sc:])

        def both(fa, fb):
            if fa is None and fb is None:
                return None

            def run():
                for fn in (fa, fb):
                    if fn is not None:
                        fn()
            return run

        return tuple(both(fa, fb) for fa, fb in zip(pa, pb))

    return _Carry(list(a.ins) + list(b.ins), list(a.outs) + list(b.outs), aliases,
                  list(a.scratch) + list(b.scratch), phases, a.peers if a.peers == b.peers else "both")


def _pcall(body, *, name, grid, in_specs, out_specs, out_shape, scratch=(), aliases=None, vmem_mb=48,
           carry=None):
    params = pltpu.CompilerParams(dimension_semantics=("arbitrary",) * len(grid),
                                  vmem_limit_bytes=vmem_mb << 20)
    if carry is None:
        plain = pl.pallas_call(
            body, name=name, grid=grid, in_specs=in_specs, out_specs=out_specs, out_shape=_out_hbm(out_shape),
            scratch_shapes=list(scratch), input_output_aliases=aliases or {}, compiler_params=params)
        return lambda *args: plain(*_in_hbm(args))
    single = not isinstance(out_shape, (list, tuple))
    out_specs_l = [out_specs] if single else list(out_specs)
    out_shape_l = [out_shape] if single else list(out_shape)
    n_in, n_out, n_sc = len(in_specs), len(out_shape_l), len(scratch)
    k_in, k_out = len(carry.ins), len(carry.outs)
    nsteps = int(np.prod(grid))
    assert nsteps >= 3

    def wrapped(*refs):
        ins, cins = refs[:n_in], refs[n_in:n_in + k_in]
        o0 = n_in + k_in
        outs, couts = refs[o0:o0 + n_out], refs[o0 + n_out:o0 + n_out + k_out]
        s0 = o0 + n_out + k_out
        sc, csc = refs[s0:s0 + n_sc], refs[s0 + n_sc:]
        step = pl.program_id(0)
        for ax in range(1, len(grid)):
            step = step * grid[ax] + pl.program_id(ax)
        start, mid, end = carry.phases(cins, couts, csc)

        @pl.when(step == 0)
        def _():
            _peer_barrier(carry.peers)
            start()

        body(*ins, *outs, *sc)
        if mid is not None:
            pl.when(step == nsteps - 2)(mid)
        pl.when(step == nsteps - 1)(end)

    all_aliases = dict(aliases or {})
    all_aliases.update({n_in + i: n_out + o for i, o in carry.aliases.items()})
    call = pl.pallas_call(
        wrapped, name=name, grid=grid, in_specs=list(in_specs) + [ANY] * k_in,
        out_specs=out_specs_l + [ANY] * k_out, out_shape=_out_hbm(out_shape_l + list(carry.outs)),
        scratch_shapes=list(scratch) + list(carry.scratch), input_output_aliases=all_aliases,
        compiler_params=pltpu.CompilerParams(dimension_semantics=("arbitrary",) * len(grid),
                                             vmem_limit_bytes=vmem_mb << 20,
                                             collective_id=BARRIER_IDS[carry.peers]))

    def run(*args):
        res = call(*_in_hbm(args), *_in_hbm(carry.ins))
        core = res[:n_out]
        return (core[0] if single else list(core)), list(res[n_out:])

    return run


def _full(shape):
    nd = len(shape)
    return pl.BlockSpec(shape, lambda *_: (0,) * nd)


ANY = pl.BlockSpec(memory_space=pl.ANY)


NT_IN = NCOL // 256


def _src_block(j):
    return j + jnp.where(j < 4, 2, jnp.where(j < 6, 3, jnp.where(j < 8, -6, jnp.where(
        j < 16, 5, jnp.where(j < 20, -7, -14)))))


def _mm_in(h, wt, tm, carry=None):
    tt = h.shape[0]

    def body(h_ref, w_ref, o_ref):
        o_ref[...] = _bdot(h_ref[...], w_ref[...], NT)

    return _pcall(body, name="mm_in", grid=(tt // tm, NT_IN),
                  in_specs=[pl.BlockSpec((tm, D), lambda i, j: (i, 0)),
                            pl.BlockSpec((256, D), lambda i, j: (_src_block(j), 0))],
                  out_specs=pl.BlockSpec((tm, 256), lambda i, j: (i, j)),
                  out_shape=jax.ShapeDtypeStruct((tt, NCOL), F32), carry=carry)(h, wt)


def _mm_dh(dp, wt, tm, carry=None):
    tt = dp.shape[0]
    per, ng = 3, NT_IN // 3

    def body(d_ref, w0, w1, w2, o_ref, acc):
        kk = pl.program_id(1)

        @pl.when(kk == 0)
        def _():
            acc[...] = jnp.zeros_like(acc)

        acc[...] += (_bdot(d_ref[:, 0:256], w0[...]) + _bdot(d_ref[:, 256:512], w1[...])
                     + _bdot(d_ref[:, 512:768], w2[...]))

        @pl.when(kk == ng - 1)
        def _():
            o_ref[...] = acc[...]

    wspecs = [pl.BlockSpec((256, D), functools.partial(lambda t, i, kk: (_src_block(per * kk + t), 0), t))
              for t in range(per)]
    return _pcall(body, name="mm_dh", grid=(tt // tm, ng),
                  in_specs=[pl.BlockSpec((tm, per * 256), lambda i, kk: (i, kk))] + wspecs,
                  out_specs=pl.BlockSpec((tm, D), lambda i, kk: (i, 0)),
                  out_shape=jax.ShapeDtypeStruct((tt, D), F32), scratch=[pltpu.VMEM((tm, D), F32)],
                  carry=carry)(dp, wt, wt, wt)


def _mm_gin(dp, h, tk):
    tt = dp.shape[0]
    nk = tt // tk

    def body(d_ref, h_ref, o_ref, acc):
        kk = pl.program_id(1)

        @pl.when(kk == 0)
        def _():
            acc[...] = jnp.zeros_like(acc)

        acc[...] += _bdot(d_ref[...], h_ref[...], TN)

        @pl.when(kk == nk - 1)
        def _():
            o_ref[...] = acc[...].astype(BF16)

    return _pcall(body, name="mm_gin", grid=(NT_IN, nk),
                  in_specs=[pl.BlockSpec((tk, 256), lambda j, kk: (kk, j)),
                            pl.BlockSpec((tk, D), lambda j, kk: (kk, 0))],
                  out_specs=pl.BlockSpec((256, D), lambda j, kk: (_src_block(j), 0)),
                  out_shape=jax.ShapeDtypeStruct((NCOL, D), BF16), scratch=[pltpu.VMEM((256, D), F32)])(dp, h)


def _tok_specs():
    assert L == TM
    return [_full((TM, D)), pl.BlockSpec((TM, D), lambda i: (jnp.maximum(i - 1, 0), 0))]


def _mod1(ctx, x, nw, ss):
    rows = L + x.shape[0]

    def body(c_ref, x_ref, nw_ref, ss_ref, h_ref):
        t = jnp.where(pl.program_id(0) == 0, c_ref[...], x_ref[...])
        r = lax.rsqrt(jnp.mean(t * t, axis=-1, keepdims=True) + EPS)
        s = ss_ref[0]
        h_ref[...] = ((t * r * nw_ref[...]) * (1.0 + s[1:2]) + s[0:1]).astype(BF16)

    return _pcall(body, name="mod1", grid=(rows // TM,),
                  in_specs=_tok_specs() + [_full((1, D)),
                                           pl.BlockSpec((1, 2, D), lambda i: (jnp.minimum(i, 1), 0, 0))],
                  out_specs=pl.BlockSpec((TM, D), lambda i: (i, 0)),
                  out_shape=jax.ShapeDtypeStruct((rows, D), BF16))(ctx, x, nw, ss)


def _norm_bwd_rows(x, dh, nw, scale):
    r = lax.rsqrt(jnp.mean(x * x, axis=-1, keepdims=True) + EPS)
    xh = x * r
    dxh = dh * ((1.0 + scale) * nw)
    dx = r * (dxh - xh * jnp.mean(dxh * xh, axis=-1, keepdims=True))
    return dx, xh


def _out_proj_mod2(mixed, w_o, x, g1, nw2, ss2):
    s_len = x.shape[0]
    tm = 512

    def body(m_ref, w_ref, x_ref, g_ref, nw_ref, ss_ref, ao_ref, x1_ref, h_ref):
        ao = _bdot(m_ref[...], w_ref[...])
        ao_ref[...] = ao.astype(BF16)
        x1 = x_ref[...] + g_ref[...] * ao
        x1_ref[...] = x1
        r = lax.rsqrt(jnp.mean(x1 * x1, axis=-1, keepdims=True) + EPS)
        s = ss_ref[0]
        h_ref[...] = ((x1 * r * nw_ref[...]) * (1.0 + s[1:2]) + s[0:1]).astype(BF16)

    row = pl.BlockSpec((tm, D), lambda i: (i, 0))
    f = jax.ShapeDtypeStruct((s_len, D), F32)
    return _pcall(body, name="out_proj_mod2", grid=(s_len // tm,),
                  in_specs=[row, _full((D, D)), row, _full((1, D)), _full((1, D)), _full((1, 2, D))],
                  out_specs=[row, row, row],
                  out_shape=[jax.ShapeDtypeStruct((s_len, D), BF16), f,
                             jax.ShapeDtypeStruct((s_len, D), BF16)])(mixed, w_o, x, g1, nw2, ss2)


TS = 1024


def _acc_call(body, *, name, grid, in_specs, out_specs, out_shape, acc_shapes, args, carry=None):
    return _pcall(body, name=name, grid=grid, in_specs=in_specs, out_specs=out_specs, out_shape=out_shape,
                  scratch=[pltpu.VMEM(s, F32) for s in acc_shapes], carry=carry)(*args)


def _ffn_up(h2, g4, u4, carry=None):
    s_len = h2.shape[0]
    ns = g4.shape[1]

    def body(h_ref, g_ref, u_ref, a_ref, b_ref, z_ref):
        h = h_ref[...]
        a = _bdot(h, g_ref[0], NT)
        b = _bdot(h, u_ref[0], NT)
        a_ref[0] = a.astype(BF16)
        b_ref[0] = b.astype(BF16)
        z_ref[0] = (a * _sig(a) * b).astype(BF16)

    w = pl.BlockSpec((1, ns, D), lambda i, j: (j, 0, 0))
    o = pl.BlockSpec((1, TS, ns), lambda i, j: (j, i, 0))
    f = jax.ShapeDtypeStruct((4, s_len, ns), BF16)
    return _pcall(body, name="ffn_up", grid=(s_len // TS, 4),
                  in_specs=[pl.BlockSpec((TS, D), lambda i, j: (i, 0)), w, w], out_specs=[o, o, o],
                  out_shape=[f, f, jax.ShapeDtypeStruct((4, s_len, ns), BF16)], carry=carry)(h2, g4, u4)


def _ffn_down_loss(z4, dn4, x1, g2, tgt):
    _, s_len, ns = z4.shape

    def body(z_ref, w_ref, x1_ref, g_ref, t_ref, sq_ref, dx2_ref, dyb_ref, dg_ref, acc):
        i, j = pl.program_id(0), pl.program_id(1)

        @pl.when((i == 0) & (j == 0))
        def _():
            sq_ref[...] = jnp.zeros_like(sq_ref)
            dg_ref[...] = jnp.zeros_like(dg_ref)

        @pl.when(j == 0)
        def _():
            acc[...] = jnp.zeros_like(acc)

        acc[...] += _bdot(z_ref[0], w_ref[0])

        @pl.when(j == 3)
        def _():
            y_ = acc[...]
            g = g_ref[...]
            e = x1_ref[...] + g * y_ - t_ref[...]
            sq_ref[...] += jnp.sum(e * e, axis=0, keepdims=True)
            dx2 = e * (1.0 / D)
            dx2_ref[...] = dx2
            dyb_ref[...] = (g * dx2).astype(BF16)
            dg_ref[...] += jnp.sum(dx2 * y_, axis=0, keepdims=True)

    row = pl.BlockSpec((TS, D), lambda i, j: (i, 0))
    vec = _full((1, D))
    return _acc_call(body, name="ffn_down_loss", grid=(s_len // TS, 4),
                     in_specs=[pl.BlockSpec((1, TS, ns), lambda i, j: (j, i, 0)),
                               pl.BlockSpec((1, ns, D), lambda i, j: (j, 0, 0)), row, vec, row],
                     out_specs=[vec, row, row, vec],
                     out_shape=[jax.ShapeDtypeStruct((1, D), F32), jax.ShapeDtypeStruct((s_len, D), F32),
                                jax.ShapeDtypeStruct((s_len, D), BF16), jax.ShapeDtypeStruct((1, D), F32)],
                     acc_shapes=[(TS, D)], args=(z4, dn4, x1, g2, tgt))


def _ffn_dz(dyb, dn4, a4, b4):
    _, s_len, ns = a4.shape

    def body(dy_ref, w_ref, a_ref, b_ref, da_ref, db_ref):
        dz = _bdot(dy_ref[...], w_ref[0], NT)
        a = a_ref[0].astype(F32)
        s = _sig(a)
        da_ref[0] = (dz * b_ref[0].astype(F32) * (s * (1.0 + a * (1.0 - s)))).astype(BF16)
        db_ref[0] = (dz * (a * s)).astype(BF16)

    t = pl.BlockSpec((1, TS, ns), lambda i, j: (j, i, 0))
    o = jax.ShapeDtypeStruct((4, s_len, ns), BF16)
    return _pcall(body, name="ffn_dz", grid=(s_len // TS, 4),
                  in_specs=[pl.BlockSpec((TS, D), lambda i, j: (i, 0)),
                            pl.BlockSpec((1, ns, D), lambda i, j: (j, 0, 0)), t, t],
                  out_specs=[t, t], out_shape=[o, o])(dyb, dn4, a4, b4)


def _ffn_gdn(z4, dyb):
    _, s_len, ns = z4.shape
    tk = min(s_len, 2 * TS)
    nk = s_len // tk

    def body(z_ref, dy_ref, o_ref, acc):
        t = pl.program_id(1)

        @pl.when(t == 0)
        def _():
            acc[...] = jnp.zeros_like(acc)

        acc[...] += _bdot(z_ref[0], dy_ref[...], TN)

        @pl.when(t == nk - 1)
        def _():
            o_ref[0] = acc[...].astype(o_ref.dtype)

    return _acc_call(body, name="ffn_gdn", grid=(4, nk),
                     in_specs=[pl.BlockSpec((1, tk, ns), lambda j, t: (j, t, 0)),
                               pl.BlockSpec((tk, D), lambda j, t: (t, 0))],
                     out_specs=pl.BlockSpec((1, ns, D), lambda j, t: (j, 0, 0)),
                     out_shape=jax.ShapeDtypeStruct((4, ns, D), BF16), acc_shapes=[(ns, D)], args=(z4, dyb))


def _ffn_dh2(da4, db4, g4, u4, carry=None):
    _, s_len, ns = da4.shape

    def body(da_ref, db_ref, g_ref, u_ref, o_ref, acc):
        j = pl.program_id(1)

        @pl.when(j == 0)
        def _():
            acc[...] = jnp.zeros_like(acc)

        acc[...] += _bdot(da_ref[0], g_ref[0]) + _bdot(db_ref[0], u_ref[0])

        @pl.when(j == 3)
        def _():
            o_ref[...] = acc[...]

    t = pl.BlockSpec((1, TS, ns), lambda i, j: (j, i, 0))
    w = pl.BlockSpec((1, ns, D), lambda i, j: (j, 0, 0))
    return _acc_call(body, name="ffn_dh2", grid=(s_len // TS, 4), in_specs=[t, t, w, w],
                     out_specs=pl.BlockSpec((TS, D), lambda i, j: (i, 0)),
                     out_shape=jax.ShapeDtypeStruct((s_len, D), F32), acc_shapes=[(TS, D)],
                     args=(da4, db4, g4, u4), carry=carry)


def _ffn_ggu(h2, da4, db4, carry=None):
    _, s_len, ns = da4.shape
    nk = s_len // TS

    def body(h_ref, da_ref, db_ref, gg_ref, gu_ref, acc_g, acc_u):
        t = pl.program_id(1)

        @pl.when(t == 0)
        def _():
            acc_g[...] = jnp.zeros_like(acc_g)
            acc_u[...] = jnp.zeros_like(acc_u)

        h = h_ref[...]
        acc_g[...] += _bdot(da_ref[0], h, TN)
        acc_u[...] += _bdot(db_ref[0], h, TN)

        @pl.when(t == nk - 1)
        def _():
            gg_ref[0] = acc_g[...].astype(BF16)
            gu_ref[0] = acc_u[...].astype(BF16)

    d = pl.BlockSpec((1, TS, ns), lambda j, t: (j, t, 0))
    o = pl.BlockSpec((1, ns, D), lambda j, t: (j, 0, 0))
    f = jax.ShapeDtypeStruct((4, ns, D), BF16)
    return _acc_call(body, name="ffn_ggu", grid=(4, nk),
                     in_specs=[pl.BlockSpec((TS, D), lambda j, t: (t, 0)), d, d], out_specs=[o, o],
                     out_shape=[f, f], acc_shapes=[(ns, D), (ns, D)], args=(h2, da4, db4), carry=carry)


def _mod2_bwd(x1, dh2, dx2, ao, nw2, ss2, g1):
    s_len = x1.shape[0]

    def body(x1_ref, dh_ref, dx2_ref, ao_ref, nw_ref, ss_ref, g_ref,
             dx1_ref, da_ref, dss_ref, dnw_ref, dg_ref):
        i = pl.program_id(0)

        @pl.when(i == 0)
        def _():
            dss_ref[...] = jnp.zeros_like(dss_ref)
            dnw_ref[...] = jnp.zeros_like(dnw_ref)
            dg_ref[...] = jnp.zeros_like(dg_ref)

        dh = dh_ref[...]
        nw = nw_ref[...]
        scale = ss_ref[0][1:2]
        dxn, xh = _norm_bwd_rows(x1_ref[...], dh, nw, scale)
        dx1 = dx2_ref[...] + dxn
        dx1_ref[...] = dx1
        da_ref[...] = (g_ref[...] * dx1).astype(BF16)
        dg_ref[...] += jnp.sum(dx1 * ao_ref[...].astype(F32), axis=0, keepdims=True)
        dsh = jnp.sum(dh, axis=0, keepdims=True)
        dsc = jnp.sum(dh * xh * nw, axis=0, keepdims=True)
        dss_ref[...] += jnp.concatenate([dsh, dsc], axis=0)
        dnw_ref[...] += jnp.sum(dh * xh * (1.0 + scale), axis=0, keepdims=True)

    row = pl.BlockSpec((TM, D), lambda i: (i, 0))
    vec = _full((1, D))
    return _pcall(body, name="mod2_bwd", grid=(s_len // TM,),
                  in_specs=[row, row, row, row, vec, _full((1, 2, D)), vec],
                  out_specs=[row, row, _full((2, D)), vec, vec],
                  out_shape=[jax.ShapeDtypeStruct((s_len, D), F32), jax.ShapeDtypeStruct((s_len, D), BF16),
                             jax.ShapeDtypeStruct((2, D), F32), jax.ShapeDtypeStruct((1, D), F32),
                             jax.ShapeDtypeStruct((1, D), F32)])(x1, dh2, dx2, ao, nw2, ss2, g1)


def _mod1_bwd(ctx, x, dh, dx1, nw1, ss1):
    s_len = dx1.shape[0]
    tt = L + s_len

    def body(c_ref, x_ref, dh_ref, dx1_ref, nw_ref, ss_ref, dx_ref, dss_ref, dnw_ref):
        i = pl.program_id(0)
        tok = jnp.where(i == 0, c_ref[...], x_ref[...])

        @pl.when(i == 0)
        def _():
            dnw_ref[...] = jnp.zeros_like(dnw_ref)

        @pl.when(i <= 1)
        def _():
            dss_ref[...] = jnp.zeros_like(dss_ref)

        dh_ = dh_ref[...]
        nw = nw_ref[...]
        scale = ss_ref[0][1:2]
        dxn, xh = _norm_bwd_rows(tok, dh_, nw, scale)

        @pl.when(i >= 1)
        def _():
            dx_ref[...] = dx1_ref[...] + dxn

        dsh = jnp.sum(dh_, axis=0, keepdims=True)
        dsc = jnp.sum(dh_ * xh * nw, axis=0, keepdims=True)
        dss_ref[...] += jnp.concatenate([dsh, dsc], axis=0)[None]
        dnw_ref[...] += jnp.sum(dh_ * xh * (1.0 + scale), axis=0, keepdims=True)

    row = pl.BlockSpec((TM, D), lambda i: (i, 0))
    lat = pl.BlockSpec((TM, D), lambda i: (jnp.maximum(i - 1, 0), 0))
    sel = pl.BlockSpec((1, 2, D), lambda i: (jnp.minimum(i, 1), 0, 0))
    return _pcall(body, name="mod1_bwd", grid=(tt // TM,),
                  in_specs=_tok_specs() + [row, lat, _full((1, D)), sel],
                  out_specs=[lat, sel, _full((1, D))],
                  out_shape=[jax.ShapeDtypeStruct((s_len, D), F32), jax.ShapeDtypeStruct((2, 2, D), F32),
                             jax.ShapeDtypeStruct((1, D), F32)])(ctx, x, dh, dx1, nw1, ss1)


def _rows(c):
    return slice(c * CH, (c + 1) * CH)


def _chunk_masks(rev, transpose=False):
    r = lax.broadcasted_iota(jnp.int32, (TM, TM), 0)
    c = lax.broadcasted_iota(jnp.int32, (TM, TM), 1)
    same = (r // CH) == (c // CH)
    before = (c >= r) if (rev != transpose) else (c <= r)
    return same & before, same


def _chunk_scan(x, rev, transpose=False):
    r = lax.broadcasted_iota(jnp.int32, (CH, CH), 0)
    c = lax.broadcasted_iota(jnp.int32, (CH, CH), 1)
    tri = ((c >= r) if (rev != transpose) else (c <= r)).astype(F32)
    return jnp.concatenate([_dot(tri, x[_rows(ch)], prec=HI) for ch in range(x.shape[0] // CH)], axis=0)


def _chunk_total(x):
    return jnp.concatenate([jnp.broadcast_to(jnp.sum(x[_rows(ch)], axis=0, keepdims=True), (CH, x.shape[1]))
                            for ch in range(x.shape[0] // CH)], axis=0)


def _hgrn_gate(fl, qraw, lg):
    lb = 1.0 / (1.0 + jnp.exp(lg[1:2] - lg[0:1]))
    sg = _sig(fl)
    f = lb + (1.0 - lb) * sg
    q = qraw * _sig(qraw) * (HGD ** -0.5)
    return lb, sg, f, q


def _hgrn_fwd(p, lg, *, rev, carry=None, readout=None):
    tt = p.shape[0]
    nt = tt // TM
    ncht = TM // CH
    d = 1 if rev else 0

    def tile_of(s):
        return jnp.where(s == 0, 0, nt - s) if rev else s

    def body(*refs):
        if readout is None:
            f_ref, inp_ref, q_ref, lg_ref, o_ref, st_ref, state = refs
        else:
            f_ref, inp_ref, q_ref, lg_ref, oo_ref, g_ref, hw_ref, o_ref, st_ref, y_ref, state = refs
        s = pl.program_id(0)

        @pl.when(s == 0)
        def _():
            state[...] = jnp.zeros_like(state)

        _, _, f, q = _hgrn_gate(f_ref[...], q_ref[...], lg_ref[0])
        lf = jnp.log(f)
        causal, _ = _chunk_masks(rev)
        cum = _chunk_scan(lf, rev)
        tot = _chunk_total(lf)
        qd = (q * jnp.exp(cum)).astype(BF16)
        kd = ((1.0 - f) * jnp.exp(-cum)).astype(BF16)
        ke = ((1.0 - f) * jnp.exp(tot - cum)).astype(BF16)
        et = jnp.exp(tot)
        v = inp_ref[...].astype(BF16)
        order = range(ncht - 1, -1, -1) if rev else range(ncht)
        outs = []
        for h in range(4):
            sl = slice(h * HGD, (h + 1) * HGD)
            qd_, kd_, ke_, v_ = qd[:, sl], kd[:, sl], ke[:, sl], v[:, sl]
            pm = jnp.where(causal, _dot(qd_, kd_, NT), 0.0).astype(BF16)
            o_h = _dot(pm, v_)
            upd = [_dot(v_[_rows(c)], ke_[_rows(c)], TN) for c in range(ncht)]
            st = state[h]
            for c in order:
                st_ref[c, h] = st
                st = st * et[c * CH:c * CH + 1, sl] + upd[c]
            state[h] = st
            inter = [_dot(qd_[_rows(c)], st_ref[c, h].astype(BF16), NT) for c in range(ncht)]
            outs.append(o_h + jnp.concatenate(inter, axis=0))
        o_tile = jnp.concatenate(outs, axis=1)
        o_ref[...] = o_tile
        if readout is not None:
            @pl.when(tile_of(s) >= 1)
            def _():
                g = g_ref[...]
                y_ref[...] = (_head_rms(oo_ref[...] + o_tile, None, 4) * hw_ref[...] * (g * _sig(g))).astype(BF16)

    def col(cb):
        return pl.BlockSpec((TM, HGW), lambda s: (tile_of(s), cb))

    in_specs = [col(C_FB if rev else C_FF), col(C_INP), col(C_QHG), pl.BlockSpec((1, 2, HGW), lambda s: (d, 0, 0))]
    out_specs = [col(0), pl.BlockSpec((ncht, 4, HGD, HGD), lambda s: (tile_of(s), 0, 0, 0))]
    out_shape = [jax.ShapeDtypeStruct((tt, HGW), F32), jax.ShapeDtypeStruct((nt * ncht, 4, HGD, HGD), F32)]
    args = [p, p, p, lg]
    if readout is not None:
        in_specs += [col(0), col(C_GHG), _full((1, HGW))]
        args += [readout[0], p, readout[1]]
        assert rev
        out_specs.append(pl.BlockSpec((TM, HGW), lambda s: (jnp.where(s == 0, nt - 2, tile_of(s) - 1), 0)))
        out_shape.append(jax.ShapeDtypeStruct((tt - L, HGW), BF16))
    return _pcall(body, name="hgrn_fwd_rev" if rev else "hgrn_fwd", grid=(nt,), in_specs=in_specs,
                  out_specs=out_specs, out_shape=out_shape, scratch=[pltpu.VMEM((4, HGD, HGD), F32)],
                  carry=carry)(*args)


def _hgrn_bwd(p, lg, do, st, dp, prev, *, rev, carry=None):
    tt = p.shape[0]
    nt = tt // TM
    ncht = TM // CH
    d = 1 if rev else 0
    second = prev is not None

    def tile_of(s):
        return jnp.where(s == nt - 1, 0, s + 1) if rev else nt - 1 - s

    def body(*refs):
        if second:
            (f_ref, inp_ref, q_ref, lg_ref, do_ref, st_ref, dvp_ref, dqp_ref, _dp_in,
             dp_ref, dlg_ref, dstate) = refs
        else:
            (f_ref, inp_ref, q_ref, lg_ref, do_ref, st_ref, _dp_in,
             dp_ref, dv_ref, dq_ref, dlg_ref, dstate) = refs
        s = pl.program_id(0)
        tile = tile_of(s)

        @pl.when(s == 0)
        def _():
            dstate[...] = jnp.zeros_like(dstate)
            dlg_ref[...] = jnp.zeros_like(dlg_ref)

        qraw = q_ref[...]
        lb, sg, f, q = _hgrn_gate(f_ref[...], qraw, lg_ref[0])
        lf = jnp.log(f)
        causal, _ = _chunk_masks(rev)
        causal_t, _ = _chunk_masks(rev, transpose=True)
        cum = _chunk_scan(lf, rev)
        tot = _chunk_total(lf)
        ea, eb, ee, et = jnp.exp(cum), jnp.exp(-cum), jnp.exp(tot - cum), jnp.exp(tot)
        qdf, kdf, kef = q * ea, (1.0 - f) * eb, (1.0 - f) * ee
        qd, kd, ke = qdf.astype(BF16), kdf.astype(BF16), kef.astype(BF16)
        v = inp_ref[...].astype(BF16)
        dob = jnp.where(tile == 0, 0.0, do_ref[...]).astype(BF16)
        order = range(ncht) if rev else range(ncht - 1, -1, -1)
        dq_l, dk_l, dv_l, dcum_l, dtot_l = [], [], [], [], []
        for h in range(4):
            sl = slice(h * HGD, (h + 1) * HGD)
            qd_, kd_, ke_, v_, do_ = qd[:, sl], kd[:, sl], ke[:, sl], v[:, sl], dob[:, sl]
            pmt = jnp.where(causal_t, _dot(kd_, qd_, NT), 0.0).astype(BF16)
            dpm = jnp.where(causal, _dot(do_, v_, NT), 0.0).astype(BF16)
            dpmt = jnp.where(causal_t, _dot(v_, do_, NT), 0.0).astype(BF16)
            dv = _dot(pmt, do_)
            dqd = _dot(dpm, kd_)
            dkd = _dot(dpmt, qd_)
            upd = [_dot(do_[_rows(c)], qd_[_rows(c)], TN) for c in range(ncht)]
            ds = dstate[h]
            ds1 = [None] * ncht
            for c in order:
                ds1[c] = ds
                ds = ds * et[c * CH:c * CH + 1, sl] + upd[c]
            dstate[h] = ds
            dke_c, dv_c, dqd_c, dtot_c = [], [], [], []
            for c in range(ncht):
                st0 = st_ref[c, h]
                dsb = ds1[c].astype(BF16)
                dke_ = _dot(v_[_rows(c)], dsb)
                dke_c.append(dke_)
                dv_c.append(_dot(ke_[_rows(c)], dsb, NT))
                dqd_c.append(_dot(do_[_rows(c)], st0.astype(BF16)))
                dt = (jnp.sum(ds1[c] * st0, axis=0, keepdims=True) * et[c * CH:c * CH + 1, sl]
                      + jnp.sum(dke_ * kef[_rows(c), sl], axis=0, keepdims=True))
                dtot_c.append(jnp.broadcast_to(dt, (CH, HGD)))
            dke = jnp.concatenate(dke_c, axis=0)
            dqd = dqd + jnp.concatenate(dqd_c, axis=0)
            dv_l.append(dv + jnp.concatenate(dv_c, axis=0))
            dtot_l.append(jnp.concatenate(dtot_c, axis=0))
            dq_l.append(dqd * ea[:, sl])
            dk_l.append(dkd * eb[:, sl] + dke * ee[:, sl])
            dcum_l.append(dqd * qdf[:, sl] - dkd * kdf[:, sl] - dke * kef[:, sl])
        dcum = jnp.concatenate(dcum_l, axis=1)
        dlf = _chunk_scan(dcum, rev, transpose=True) + jnp.concatenate(dtot_l, axis=1)
        dq_t = jnp.concatenate(dq_l, axis=1)
        dv_t = jnp.concatenate(dv_l, axis=1)

        df = dlf / f - jnp.concatenate(dk_l, axis=1)
        dfl = df * (1.0 - lb) * sg * (1.0 - sg)
        dlb = jnp.sum(df * (1.0 - sg), axis=0, keepdims=True)
        dl0 = dlb * lb * (1.0 - lb)
        dlg_ref[...] += jnp.concatenate([dl0, -dl0], axis=0)[None]
        if second:
            sq = _sig(qraw)
            dqr = (dqp_ref[...] + dq_t) * (HGD ** -0.5) * (sq * (1.0 + qraw * (1.0 - sq)))
            dp_ref[...] = jnp.concatenate([dfl, dvp_ref[...] + dv_t, dqr], axis=1).astype(BF16)
        else:
            dp_ref[...] = dfl.astype(BF16)
            dv_ref[...] = dv_t
            dq_ref[...] = dq_t

    def col(cb):
        return pl.BlockSpec((TM, HGW), lambda s: (tile_of(s), cb))

    tok = pl.BlockSpec((TM, HGW), lambda s: (tile_of(s), 0))
    in_specs = [col(C_FB if rev else C_FF), col(C_INP), col(C_QHG),
                pl.BlockSpec((1, 2, HGW), lambda s: (d, 0, 0)),
                pl.BlockSpec((TM, HGW), lambda s: (jnp.maximum(tile_of(s) - 1, 0), 0)),
                pl.BlockSpec((ncht, 4, HGD, HGD), lambda s: (tile_of(s), 0, 0, 0))]
    args = [p, p, p, lg, do, st]
    dlg_spec = _full((1, 2, HGW))
    dlg_shape = jax.ShapeDtypeStruct((1, 2, HGW), F32)
    if second:
        in_specs += [tok, tok]
        args += [prev[0], prev[1]]
        out_specs = [pl.BlockSpec((TM, 3 * HGW), lambda s: (tile_of(s), 0)), dlg_spec]
        out_shape = [jax.ShapeDtypeStruct(dp.shape, BF16), dlg_shape]
    else:
        out_specs = [pl.BlockSpec((TM, HGW), lambda s: (tile_of(s), C_FB if rev else C_FF)), tok, tok, dlg_spec]
        out_shape = [jax.ShapeDtypeStruct(dp.shape, BF16), jax.ShapeDtypeStruct((tt, HGW), F32),
                     jax.ShapeDtypeStruct((tt, HGW), F32), dlg_shape]
    in_specs.append(ANY)
    args.append(dp)
    return _pcall(body, name="hgrn_bwd_rev" if rev else "hgrn_bwd", grid=(nt,),
                  in_specs=in_specs, out_specs=out_specs, out_shape=out_shape,
                  scratch=[pltpu.VMEM((4, HGD, HGD), F32)],
                  aliases={len(args) - 1: 0}, carry=carry)(*args)


def _head_rms(o, w, nheads):
    outs = []
    for h in range(nheads):
        oh = o[:, h * HGD:(h + 1) * HGD]
        outs.append(oh * lax.rsqrt(jnp.mean(oh * oh, axis=-1, keepdims=True) + EPS))
    return jnp.concatenate(outs, axis=1)


def _readout_bwd(o0, o1, p, hw4, dy, dp, carry=None):
    tt = o0.shape[0]
    s_len = tt - L

    def body(o0_ref, o1_ref, g_ref, w_ref, dy_ref, _dp_in, dp_ref, do_ref, dw_ref):
        i = pl.program_id(0)

        @pl.when(i == 0)
        def _():
            dw_ref[...] = jnp.zeros_like(dw_ref)
            dp_ref[...] = jnp.zeros_like(dp_ref)

        @pl.when(i >= 1)
        def _():
            o = o0_ref[...] + o1_ref[...]
            g = g_ref[...]
            w = w_ref[...]
            sg = _sig(g)
            dy_ = dy_ref[...]
            dsw = dy_ * (g * sg)
            outs, xhs = [], []
            for h in range(4):
                sl = slice(h * HGD, (h + 1) * HGD)
                oh = o[:, sl]
                r = lax.rsqrt(jnp.mean(oh * oh, axis=-1, keepdims=True) + EPS)
                xh = oh * r
                dxh = dsw[:, sl] * w[:, sl]
                outs.append(r * (dxh - xh * jnp.mean(dxh * xh, axis=-1, keepdims=True)))
                xhs.append(xh)
            xh = jnp.concatenate(xhs, axis=1)
            do_ref[...] = jnp.concatenate(outs, axis=1)
            dp_ref[...] = (dy_ * xh * w * (sg * (1.0 + g * (1.0 - sg)))).astype(BF16)
            dw_ref[...] += jnp.sum(dsw * xh, axis=0, keepdims=True)

    tok = pl.BlockSpec((TM, HGW), lambda i: (i, 0))
    lat = pl.BlockSpec((TM, HGW), lambda i: (jnp.maximum(i - 1, 0), 0))
    return _pcall(body, name="readout_bwd", grid=(tt // TM,),
                  in_specs=[tok, tok, pl.BlockSpec((TM, HGW), lambda i: (i, C_GHG)), _full((1, HGW)), lat, ANY],
                  out_specs=[pl.BlockSpec((TM, HGW), lambda i: (i, C_GHG)), lat, _full((1, HGW))],
                  out_shape=[jax.ShapeDtypeStruct(dp.shape, BF16), jax.ShapeDtypeStruct((s_len, HGW), F32),
                             jax.ShapeDtypeStruct((1, HGW), F32)],
                  aliases={5: 0}, carry=carry)(o0, o1, p, hw4, dy, dp)


def _rope_tables(s_len):
    t = np.arange(s_len)
    inv = ROPE_THETA ** (-np.arange(0, 32, 2, dtype=np.float64) / 32)
    def half(pos):
        ang = pos[:, None].astype(np.float64) * inv[None, :]
        return (np.concatenate([np.cos(ang), np.cos(ang)], 1), np.concatenate([-np.sin(ang), np.sin(ang)], 1))
    cr, sr = half(t // GRID_W)
    cc, sc = half(t % GRID_W)
    cos = np.concatenate([cr, cc, cr, cc], 1)
    sin = np.concatenate([sr, sc, sr, sc], 1)
    cos = np.concatenate([np.ones((L, 128)), cos], 0)
    sin = np.concatenate([np.zeros((L, 128)), sin], 0)
    return jnp.asarray(cos, F32), jnp.asarray(sin, F32)


def _blockdiag(n, w):
    i = np.arange(n)
    return jnp.asarray((i[:, None] // w == i[None, :] // w) / float(w), F32)


def _dup_matrix():
    m = np.zeros((128, 512), np.float32)
    for g in range(2):
        for j in range(4):
            for dd in range(HDIM):
                m[64 * g + dd, 256 * g + 64 * j + dd] = 1.0
    return m


def _head_mean(x, blockdiag):
    return _dot(x, blockdiag, prec=lax.Precision.HIGH)


def _rot(x):
    n = x.shape[1]
    lane = lax.broadcasted_iota(jnp.int32, x.shape, 1)
    return jnp.where((lane % 32) < 16, pltpu.roll(x, n - 16, 1), pltpu.roll(x, 16, 1))


def _qk_prep(p, cos, sin, qnw8, knw2, bd512, bd128, dup):
    tt = p.shape[0]

    def body(q_ref, kv_ref, cos_ref, sin_ref, qw_ref, kw_ref, b5_ref, b1_ref, dup_ref,
             qr_ref, k4_ref, v4_ref):
        cos_, sin_ = cos_ref[...], sin_ref[...]
        q = q_ref[...]
        qn = q * lax.rsqrt(_head_mean(q * q, b5_ref[...]) + EPS) * qw_ref[...]
        cos4 = jnp.concatenate([cos_] * 4, axis=1)
        sin4 = jnp.concatenate([sin_] * 4, axis=1)
        qr_ref[...] = ((qn * cos4 + _rot(qn) * sin4) * (HDIM ** -0.5)).astype(BF16)
        kv = kv_ref[...]
        k, v = kv[:, :128], kv[:, 128:]
        kn = k * lax.rsqrt(_head_mean(k * k, b1_ref[...]) + EPS) * kw_ref[...]
        kr = kn * cos_ + _rot(kn) * sin_
        k4_ref[...] = _bdot(kr, dup_ref[...]).astype(BF16)
        v4_ref[...] = _bdot(v, dup_ref[...]).astype(BF16)

    row = lambda w, cb: pl.BlockSpec((TM, w), lambda i: (i, cb))
    out = jax.ShapeDtypeStruct((tt, ATW), BF16)
    return _pcall(body, name="qk_prep", grid=(tt // TM,),
                  in_specs=[row(ATW, C_QRAW), row(256, C_KV), row(128, 0), row(128, 0),
                            _full((1, ATW)), _full((1, 128)), _full((ATW, ATW)), _full((128, 128)),
                            _full((128, ATW))],
                  out_specs=[row(ATW, 0)] * 3, out_shape=[out] * 3)(
                      p, p, cos, sin, qnw8, knw2, bd512, bd128, dup)


def _attn_masks(i, nb):
    r = lax.broadcasted_iota(jnp.int32, (4 * BLK, 3 * BLK + L), 0) % BLK
    c = lax.broadcasted_iota(jnp.int32, (4 * BLK, 3 * BLK + L), 1)
    kpos = (i - 1) * BLK + c
    loc = (jnp.abs(c - BLK - r) <= BLK) & (kpos >= 0) & (kpos < nb * BLK)
    return loc | (c >= 3 * BLK)


def _stack_mask():
    r = lax.broadcasted_iota(jnp.int32, (4 * BLK, 256), 0)
    lane = lax.broadcasted_iota(jnp.int32, (4 * BLK, 256), 1)
    return (r // BLK) == (lane // HDIM)


def _stack_heads(xg, fill=0.0):
    x4 = jnp.concatenate([xg] * 4, axis=0)
    return jnp.where(_stack_mask(), x4, jnp.full_like(x4, fill))


def _unstack_heads(x4):
    out = jnp.where(_lane_mask(0), x4[0:BLK], 0.0)
    for j in range(1, 4):
        out = out + jnp.where(_lane_mask(j), x4[j * BLK:(j + 1) * BLK], 0.0)
    return out


def _per_head_rows(vals):
    return jnp.concatenate([jnp.broadcast_to(v, (BLK, 1)) for v in vals], axis=0)


def _lane_mask(j):
    lane = lax.broadcasted_iota(jnp.int32, (1, 256), 1)
    return (lane // HDIM) == j


def _attn_specs(nb):
    blk = lambda off: pl.BlockSpec((BLK, ATW), lambda i: (jnp.clip(i + off, 0, nb - 1) + 2, 0))
    ctx = pl.BlockSpec((L, ATW), lambda i: (0, 0))
    return blk, ctx


def _attn_fwd(qr, k4, v4, sinks, carry=None):
    tt = qr.shape[0]
    s_len = tt - L
    nb = s_len // BLK

    def body(sk_ref, q_ref, kp, ko, kn, kc, vp, vo, vn, vc, y_ref, lse_ref):
        i = pl.program_id(0)
        valid = _attn_masks(i, nb)
        q = q_ref[...]
        ys, lses = [], []
        for g in range(2):
            gs = slice(256 * g, 256 * g + 256)
            kcat = jnp.concatenate([kp[:, gs], ko[:, gs], kn[:, gs], kc[:, gs]], axis=0)
            vcat = jnp.concatenate([vp[:, gs], vo[:, gs], vn[:, gs], vc[:, gs]], axis=0)
            sink4 = _per_head_rows([sk_ref[4 * g + j] for j in range(4)])
            q4 = _stack_heads(q[:, gs])
            o_parts, l_parts = [], []
            for hp in range(2):
                rows = slice(2 * BLK * hp, 2 * BLK * (hp + 1))
                sink = sink4[rows]
                s = jnp.where(valid[rows], _dot(q4[rows], kcat, NT), -1e30)
                m = jnp.maximum(jnp.max(s, axis=-1, keepdims=True), sink)
                e = jnp.exp(s - m)
                den = jnp.sum(e, axis=-1, keepdims=True) + jnp.exp(sink - m)
                o_parts.append(_bdot(e * (1.0 / den), vcat))
                l_parts.append(jnp.broadcast_to(m + jnp.log(den), (2 * BLK, 256)))
            ys.append(_unstack_heads(jnp.concatenate(o_parts, axis=0)))
            lses.append(_unstack_heads(jnp.concatenate(l_parts, axis=0)))
        y_ref[...] = jnp.concatenate(ys, axis=1).astype(BF16)
        lse_ref[...] = jnp.concatenate(lses, axis=1)

    blk, ctx = _attn_specs(nb)
    out = pl.BlockSpec((BLK, ATW), lambda i: (i, 0))
    return _pcall(body, name="attn_fwd", grid=(nb,),
                  in_specs=[pl.BlockSpec(memory_space=pltpu.SMEM), blk(0),
                            blk(-1), blk(0), blk(1), ctx, blk(-1), blk(0), blk(1), ctx],
                  out_specs=[out, out],
                  out_shape=[jax.ShapeDtypeStruct((s_len, ATW), BF16),
                             jax.ShapeDtypeStruct((s_len, ATW), F32)], carry=carry)(
                      sinks, qr, k4, k4, k4, k4, v4, v4, v4, v4)


def _attn_bwd(qr, k4, v4, sinks, y, lse, dy, carry=None):
    tt = qr.shape[0]
    s_len = tt - L
    nb = s_len // BLK

    def body(sk_ref, q_ref, kp, ko, kn, kc, vp, vo, vn, vc, y_ref, lse_ref, dy_ref,
             dq_ref, dkw_ref, dvw_ref, dkc_ref, dvc_ref, dsk_ref):
        i = pl.program_id(0)

        @pl.when(i == 0)
        def _():
            dkc_ref[...] = jnp.zeros_like(dkc_ref)
            dvc_ref[...] = jnp.zeros_like(dvc_ref)
            dsk_ref[...] = jnp.zeros_like(dsk_ref)

        valid = _attn_masks(i, nb)
        q = q_ref[...]
        dy_ = dy_ref[...]
        dly = dy_ * y_ref[...].astype(F32)
        lse_ = lse_ref[...]
        dqs = []
        for g in range(2):
            gs = slice(256 * g, 256 * g + 256)
            kcat = jnp.concatenate([kp[:, gs], ko[:, gs], kn[:, gs], kc[:, gs]], axis=0)
            vcat = jnp.concatenate([vp[:, gs], vo[:, gs], vn[:, gs], vc[:, gs]], axis=0)
            q4 = _stack_heads(q[:, gs])
            dy4 = _stack_heads(dy_[:, gs]).astype(BF16)
            lse4 = jnp.max(_stack_heads(lse_[:, gs], fill=-1e30), axis=-1, keepdims=True)
            delta = jnp.sum(_stack_heads(dly[:, gs]), axis=-1, keepdims=True)
            sink = _per_head_rows([sk_ref[4 * g + j] for j in range(4)])
            pr = jnp.where(valid, jnp.exp(_dot(q4, kcat, NT) - lse4), 0.0)
            dsb = (pr * (_dot(dy4, vcat, NT) - delta)).astype(BF16)
            dsink = jnp.exp(sink - lse4) * delta
            for j in range(4):
                dsk_ref[4 * g + j:4 * g + j + 1, :] += jnp.broadcast_to(
                    -jnp.sum(dsink[j * BLK:(j + 1) * BLK], axis=0, keepdims=True), (1, 128))
            dqs.append(_unstack_heads(_dot(dsb, kcat)))
            dkg = _dot(dsb, q4, TN)
            dvg = _dot(pr.astype(BF16), dy4, TN)
            dkw_ref[0, :, gs] = dkg[:3 * BLK]
            dvw_ref[0, :, gs] = dvg[:3 * BLK]
            dkc_ref[:, gs] += dkg[3 * BLK:]
            dvc_ref[:, gs] += dvg[3 * BLK:]
        dq_ref[...] = jnp.concatenate(dqs, axis=1)

    blk, ctx = _attn_specs(nb)
    out = pl.BlockSpec((BLK, ATW), lambda i: (i, 0))
    win = pl.BlockSpec((1, 3 * BLK, ATW), lambda i: (i, 0, 0))
    acc = _full((L, ATW))
    return _pcall(body, name="attn_bwd", grid=(nb,),
                  in_specs=[pl.BlockSpec(memory_space=pltpu.SMEM), blk(0),
                            blk(-1), blk(0), blk(1), ctx, blk(-1), blk(0), blk(1), ctx, out, out, out],
                  out_specs=[out, win, win, acc, acc, _full((8, 128))],
                  out_shape=[jax.ShapeDtypeStruct((s_len, ATW), F32),
                             jax.ShapeDtypeStruct((nb, 3 * BLK, ATW), F32),
                             jax.ShapeDtypeStruct((nb, 3 * BLK, ATW), F32),
                             jax.ShapeDtypeStruct((L, ATW), F32), jax.ShapeDtypeStruct((L, ATW), F32),
                             jax.ShapeDtypeStruct((8, 128), F32)], carry=carry)(
                      sinks, qr, k4, k4, k4, k4, v4, v4, v4, v4, y, lse, dy)


def _attn_post(p, cos, sin, qnw8, knw2, bd512, bd128, dupt, dq, dkw, dvw, dkc, dvc, dp, carry=None):
    tt = p.shape[0]
    s_len = tt - L
    nb = s_len // BLK
    nctx = L // BLK

    def body(q_ref, kv_ref, cos_ref, sin_ref, qw_ref, kw_ref, b5_ref, b1_ref, dupt_ref,
             dq_ref, kwp, kwo, kwn, vwp, vwo, vwn, dkc_ref, dvc_ref, _dp_in,
             dp_ref, dqw_ref, dkw_ref):
        t = pl.program_id(0)
        j = t - nctx

        @pl.when(t == 0)
        def _():
            dqw_ref[...] = jnp.zeros_like(dqw_ref)
            dkw_ref[...] = jnp.zeros_like(dkw_ref)

        is_lat = t >= nctx
        cos_, sin_ = cos_ref[...], sin_ref[...]
        has_p = is_lat & (j >= 1)
        has_n = is_lat & (j <= nb - 2)
        dk4 = (jnp.where(is_lat, kwo[0], dkc_ref[...]) + jnp.where(has_p, kwp[0], 0.0)
               + jnp.where(has_n, kwn[0], 0.0))
        dv4 = (jnp.where(is_lat, vwo[0], dvc_ref[...]) + jnp.where(has_p, vwp[0], 0.0)
               + jnp.where(has_n, vwn[0], 0.0))
        dkr = _dot(dk4, dupt_ref[...], prec=HI)
        dv = _dot(dv4, dupt_ref[...], prec=HI)
        kv = kv_ref[...]
        k = kv[:, :128]
        kw = kw_ref[...]
        rk = lax.rsqrt(_head_mean(k * k, b1_ref[...]) + EPS)
        xk = k * rk
        dkn = dkr * cos_ + _rot(dkr * sin_)
        dxk = dkn * kw
        dk = rk * (dxk - xk * _head_mean(dxk * xk, b1_ref[...]))
        dkw_ref[...] += jnp.sum(dkn * xk, axis=0, keepdims=True)
        q = q_ref[...]
        qw = qw_ref[...]
        rq = lax.rsqrt(_head_mean(q * q, b5_ref[...]) + EPS)
        xq = q * rq
        cos4 = jnp.concatenate([cos_] * 4, axis=1)
        sin4 = jnp.concatenate([sin_] * 4, axis=1)
        dqr = jnp.where(is_lat, dq_ref[...], 0.0) * (HDIM ** -0.5)
        dqn = dqr * cos4 + _rot(dqr * sin4)
        dxq = dqn * qw
        dqraw = rq * (dxq - xq * _head_mean(dxq * xq, b5_ref[...]))
        dqw_ref[...] += jnp.sum(dqn * xq, axis=0, keepdims=True)
        dp_ref[...] = jnp.concatenate([dqraw, dk, dv], axis=1).astype(BF16)

    row = lambda w, cb: pl.BlockSpec((BLK, w), lambda t: (t, cb))
    lat = pl.BlockSpec((BLK, ATW), lambda t: (jnp.maximum(t - nctx, 0), 0))

    def part(off):
        return pl.BlockSpec((1, BLK, ATW), lambda t: (jnp.clip(t - nctx + off, 0, nb - 1), 1 - off, 0))

    cacc = pl.BlockSpec((BLK, ATW), lambda t: (jnp.minimum(t, nctx - 1), 0))
    return _pcall(body, name="attn_post", grid=(tt // BLK,),
                  in_specs=[row(ATW, C_QRAW), row(256, C_KV), row(128, 0), row(128, 0),
                            _full((1, ATW)), _full((1, 128)), _full((ATW, ATW)), _full((128, 128)),
                            _full((ATW, 128)), lat, part(-1), part(0), part(1), part(-1), part(0), part(1),
                            cacc, cacc, ANY],
                  out_specs=[pl.BlockSpec((BLK, 768), lambda t: (t, C_QKV)), _full((1, ATW)), _full((1, 128))],
                  out_shape=[jax.ShapeDtypeStruct(dp.shape, BF16), jax.ShapeDtypeStruct((1, ATW), F32),
                             jax.ShapeDtypeStruct((1, 128), F32)],
                  aliases={18: 0}, carry=carry)(p, p, cos, sin, qnw8, knw2, bd512, bd128, dupt,
                                   dq, dkw, dkw, dkw, dvw, dvw, dvw, dkc, dvc, dp)


def _branch_merge(y_hg, y_at, bh4, ba4, p):
    s_len = y_hg.shape[0]

    def body(yh_ref, ya_ref, bh_ref, ba_ref, gh_ref, ga_ref, ah_ref, aa_ref, m_ref):
        yh, ya = yh_ref[...], ya_ref[...]
        ah = jnp.concatenate([_bdot(yh, bh_ref[j]) for j in range(4)], axis=1)
        aa = jnp.concatenate([_bdot(ya, ba_ref[j]) for j in range(4)], axis=1)
        ah_ref[...] = ah.astype(BF16)
        aa_ref[...] = aa.astype(BF16)
        m_ref[...] = (_sig(gh_ref[...]) * ah + _sig(ga_ref[...]) * aa).astype(BF16)

    row = pl.BlockSpec((TM, D), lambda i: (i, 0))
    y = pl.BlockSpec((TM, HGW), lambda i: (i, 0))
    f = jax.ShapeDtypeStruct((s_len, D), BF16)
    return _pcall(body, name="branch_merge", grid=(s_len // TM,),
                  in_specs=[y, y, _full(bh4.shape), _full(ba4.shape),
                            pl.BlockSpec((TM, D), lambda i: (i + 1, 2)), pl.BlockSpec((TM, D), lambda i: (i + 1, 3))],
                  out_specs=[row, row, row],
                  out_shape=[f, f, jax.ShapeDtypeStruct((s_len, D), BF16)])(y_hg, y_at, bh4, ba4, p, p)


def _branch_bwd(dmh, dma, bh4, ba4, y_hg, y_at):
    s_len = dmh.shape[0]
    nk = s_len // TS
    ns = D // 4

    def body(dh_ref, da_ref, bh_ref, ba_ref, yh_ref, ya_ref, dyh_ref, dya_ref, gh_ref, ga_ref, acc_h, acc_a):
        t = pl.program_id(0)

        @pl.when(t == 0)
        def _():
            acc_h[...] = jnp.zeros_like(acc_h)
            acc_a[...] = jnp.zeros_like(acc_a)

        for d_ref, w_ref, y_ref, dy_ref, acc in ((dh_ref, bh_ref, yh_ref, dyh_ref, acc_h),
                                                 (da_ref, ba_ref, ya_ref, dya_ref, acc_a)):
            y = y_ref[...]
            dy = jnp.zeros((TS, HGW), F32)
            for j in range(4):
                dj = d_ref[:, j * ns:(j + 1) * ns]
                dy = dy + _bdot(dj, w_ref[j], NT)
                acc[j] += _bdot(y, dj, TN)
            dy_ref[...] = dy

        @pl.when(t == nk - 1)
        def _():
            gh_ref[...] = acc_h[...].astype(BF16)
            ga_ref[...] = acc_a[...].astype(BF16)

    dm = pl.BlockSpec((TS, D), lambda t: (t, 0))
    y = pl.BlockSpec((TS, HGW), lambda t: (t, 0))
    w = _full(bh4.shape)
    fy = jax.ShapeDtypeStruct((s_len, HGW), F32)
    gw = jax.ShapeDtypeStruct(bh4.shape, BF16)
    return _pcall(body, name="branch_bwd", grid=(nk,), in_specs=[dm, dm, w, w, y, y],
                  out_specs=[y, y, w, w], out_shape=[fy, fy, gw, gw],
                  scratch=[pltpu.VMEM(bh4.shape, F32)] * 2)(dmh, dma, bh4, ba4, y_hg, y_at)


def _merge_bwd(dattn, w_o, mixed, ah, aa, p, carry=None):
    tt = p.shape[0]
    s_len = tt - L
    nt = tt // TM

    def body(da_ref, wo_ref, mx_ref, ah_ref, aa_ref, gh_ref, ga_ref, dp_ref, dmh_ref, dma_ref, go_ref, acc):
        i = pl.program_id(0)

        @pl.when(i == 0)
        def _():
            dp_ref[...] = jnp.zeros_like(dp_ref)
            acc[...] = jnp.zeros_like(acc)

        @pl.when(i >= 1)
        def _():
            da = da_ref[...]
            acc[...] += _bdot(mx_ref[...], da, TN)
            dm_ = _bdot(da, wo_ref[...], NT)
            sh, sa = _sig(gh_ref[...]), _sig(ga_ref[...])
            dp_ref[...] = jnp.concatenate([dm_ * ah_ref[...].astype(F32) * sh * (1.0 - sh),
                                           dm_ * aa_ref[...].astype(F32) * sa * (1.0 - sa)], axis=1).astype(BF16)
            dmh_ref[...] = (dm_ * sh).astype(BF16)
            dma_ref[...] = (dm_ * sa).astype(BF16)

        @pl.when(i == nt - 1)
        def _():
            go_ref[...] = acc[...].astype(BF16)

    lat = pl.BlockSpec((TM, D), lambda i: (jnp.maximum(i - 1, 0), 0))
    return _pcall(body, name="merge_bwd", grid=(nt,),
                  in_specs=[lat, _full((D, D)), lat, lat, lat, pl.BlockSpec((TM, D), lambda i: (i, 2)),
                            pl.BlockSpec((TM, D), lambda i: (i, 3))],
                  out_specs=[pl.BlockSpec((TM, 2 * D), lambda i: (i, C_GATES)), lat, lat, _full((D, D))],
                  out_shape=[jax.ShapeDtypeStruct((tt, NCOL), BF16), jax.ShapeDtypeStruct((s_len, D), BF16),
                             jax.ShapeDtypeStruct((s_len, D), BF16), jax.ShapeDtypeStruct((D, D), BF16)],
                  scratch=[pltpu.VMEM((D, D), F32)], carry=carry)(dattn, w_o, mixed, ah, aa, p, p)


def _local_step(x, ctx, tgt, mod, modc, nw1, nw2, lg, hw, qnw, knw, sinks,
                w_in, wts, dist=None):
    s_len = x.shape[0]
    tt = s_len + L
    ss1 = jnp.stack([modc, mod[0:2]])
    ss2 = mod[3:5][None]
    g1, g2 = mod[2:3], mod[5:6]
    hw4 = jnp.tile(hw, (1, 4))
    qnw8 = jnp.tile(qnw, (1, 8))
    knw2 = jnp.tile(knw, (1, 2))
    cos, sin = _rope_tables(s_len)
    bd512, bd128 = _blockdiag(ATW, HDIM), _blockdiag(128, HDIM)
    dupm = _dup_matrix()
    dup, dupt = jnp.asarray(dupm, BF16), jnp.asarray(dupm.T, F32)
    tmt = tt

    def four(b):
        return b.reshape(4, 2 * b.shape[1], b.shape[2])

    def halves(g):
        return g.reshape(4, 2, g.shape[1] // 2, g.shape[2])

    h = _mod1(ctx, x, nw1, ss1)
    if dist is None:
        bh4, ba4, w_o, g4, u4, dn4 = wts
        p = _mm_in(h, w_in, tmt)
        o0, st0 = _hgrn_fwd(p, lg, rev=False)
        o1, st1, y_hg = _hgrn_fwd(p, lg, rev=True, readout=(o0, hw4))
    else:
        core, chip = dist
        half = wts[3].shape[1] // 2
        p, (o8, g8a) = _mm_in(h, w_in, tmt, carry=_carry_gather([wts[2], wts[3]], rows=[None, (0, half)]))
        (o0, st0), (g8,) = _hgrn_fwd(p, lg, rev=False, carry=_carry_gather([g8a], rows=[(half, half)]))
        (o1, st1, y_hg), (dn8a, bh8, ba8) = _hgrn_fwd(
            p, lg, rev=True, readout=(o0, hw4),
            carry=_carry_gather([wts[5], wts[0], wts[1]], rows=[(0, half), None, None]))
        bh4, ba4, w_o, g4 = four(bh8), four(ba8), four(o8).reshape(D, D), four(g8)
    qr, k4, v4 = _qk_prep(p, cos, sin, qnw8, knw2, bd512, bd128, dup)
    if dist is None:
        y_at, lse = _attn_fwd(qr, k4, v4, sinks)
    else:
        (y_at, lse), (u8, dn8) = _attn_fwd(qr, k4, v4, sinks,
                                           carry=_carry_gather([wts[4], dn8a], rows=[None, (half, half)]))
        u4, dn4 = four(u8), four(dn8)
    ah, aa, mixed = _branch_merge(y_hg, y_at, bh4, ba4, p)
    ao, x1, h2 = _out_proj_mod2(mixed, w_o, x, g1, nw2, ss2)
    a4, b4, z4 = _ffn_up(h2, g4, u4)
    sq, dx2, dyb, dg2 = _ffn_down_loss(z4, dn4, x1, g2, tgt)

    da4, db4 = _ffn_dz(dyb, dn4, a4, b4)
    g_dn = _ffn_gdn(z4, dyb)
    if dist is None:
        dh2 = _ffn_dh2(da4, db4, g4, u4)
    else:
        dn_units = [halves(g_dn)]
        dh2, dn_recv = _ffn_dh2(da4, db4, g4, u4, carry=_carry_pairx(dn_units))
        dn_pairs = _rs_pair_add(dn_units, dn_recv, core)
    if dist is None:
        g_g, g_u = _ffn_ggu(h2, da4, db4)
    else:
        (g_g, g_u), c_dn = _ffn_ggu(h2, da4, db4, carry=_carry_chipx(dn_pairs))
        red_dn = _rs_chip_add(dn_pairs, c_dn, core, chip)
    dx1, dattn, dss2, dnw2, dg1 = _mod2_bwd(x1, dh2, dx2, ao, nw2, ss2, g1)
    if dist is None:
        dp, dmh, dma, g_o = _merge_bwd(dattn, w_o, mixed, ah, aa, p)
    else:
        gu_units = [halves(g_g), halves(g_u)]
        (dp, dmh, dma, g_o), gu_recv = _merge_bwd(dattn, w_o, mixed, ah, aa, p, carry=_carry_pairx(gu_units))
        ffn_pairs = list(dn_pairs) + list(_rs_pair_add(gu_units, gu_recv, core))
    dy_hg, dy_at, g_bh, g_ba = _branch_bwd(dmh, dma, bh4, ba4, y_hg, y_at)
    if dist is None:
        dp, do, dhw4 = _readout_bwd(o0, o1, p, hw4, dy_hg, dp)
        dq, dkw, dvw, dkc, dvc, dsk = _attn_bwd(qr, k4, v4, sinks, y_at, lse, dy_at)
        dp, dqnw8, dknw2 = _attn_post(p, cos, sin, qnw8, knw2, bd512, bd128, dupt, dq, dkw, dvw, dkc, dvc, dp)
    else:
        mix_units = [halves(g_bh), halves(g_ba), halves(g_o.reshape(4, D // 4, D))]
        (dp, do, dhw4), mix_recv = _readout_bwd(o0, o1, p, hw4, dy_hg, dp, carry=_carry_pairx(mix_units))
        mix_pairs = _rs_pair_add(mix_units, mix_recv, core)
        (dq, dkw, dvw, dkc, dvc, dsk), bwd = _attn_bwd(
            qr, k4, v4, sinks, y_at, lse, dy_at,
            carry=_carry_join(_carry_chipx(ffn_pairs[1:2]), _carry_sibx(red_dn)))
        red_g = _rs_chip_add(ffn_pairs[1:2], bwd[0:1], core, chip)
        (dp, dqnw8, dknw2), post = _attn_post(
            p, cos, sin, qnw8, knw2, bd512, bd128, dupt, dq, dkw, dvw, dkc, dvc, dp, carry=_carry_sibx(red_g))
    if dist is None:
        dp, dv0, dq0, dlg0 = _hgrn_bwd(p, lg, do, st0, dp, None, rev=False)
        dp, dlg1 = _hgrn_bwd(p, lg, do, st1, dp, (dv0, dq0), rev=True)
    else:
        (dp, dv0, dq0, dlg0), c_u = _hgrn_bwd(p, lg, do, st0, dp, None, rev=False,
                                              carry=_carry_chipx(ffn_pairs[2:3]))
        red_u = _rs_chip_add(ffn_pairs[2:3], c_u, core, chip)
        (dp, dlg1), last = _hgrn_bwd(p, lg, do, st1, dp, (dv0, dq0), rev=True,
                                     carry=_carry_join(_carry_chipx(mix_pairs), _carry_sibx(red_u)))
        mix_reds = _rs_chip_add(mix_pairs, last[0:3], core, chip)
        ffn_done = bwd[1:2] + post[0:1] + last[3:4]
    g_in = _mm_gin(dp, h, tmt)
    if dist is None:
        dh = _mm_dh(dp, w_in, tmt)
        gx, dss1, dnw1 = _mod1_bwd(ctx, x, dh, dx1, nw1, ss1)
        rs = None
    else:
        in_units = [halves(g_in.reshape(4, NCOL // 4, D))]
        in_pairs = _rs_pair_add(in_units, _pair_exchange(in_units), core)
        first_rows, rest_rows = _split_rows(in_pairs[0].shape[1], IN_ROWS_WITH_DH)
        dh, both = _mm_dh(dp, w_in, tmt, carry=_carry_join(_carry_chipx(in_pairs, rows=first_rows),
                                                           _carry_sibx(mix_reds)))
        in_part, mix_done = both[0:1], both[1:4]
        gx, dss1, dnw1 = _mod1_bwd(ctx, x, dh, dx1, nw1, ss1)
        rs = dict(ffn_done=ffn_done, mix_done=mix_done, in_pairs=in_pairs, in_part=in_part, rest_rows=rest_rows)

    dmod = jnp.concatenate([dss1[1], dg1, dss2, dg2], axis=0)
    dmodc = dss1[0]
    raw = (dss1, dg1, dss2, dg2, dnw1, dnw2, dhw4, dqnw8, dknw2, dsk, dlg0, dlg1)
    small = dict(raw=raw, dmod=dmod, dmodc=dmodc, dnw1=dnw1, dnw2=dnw2,
                 dhw=dhw4.reshape(4, HGD).sum(0, keepdims=True),
                 dqnw=dqnw8.reshape(8, HDIM).sum(0, keepdims=True),
                 dknw=dknw2.reshape(2, HDIM).sum(0, keepdims=True),
                 dsinks=dsk[:, 0], dlg=jnp.concatenate([dlg0, dlg1], axis=0))
    big = dict(w_in=g_in, w_bh=g_bh, w_ba=g_ba, w_o=g_o, w_g=g_g, w_u=g_u, w_dn=g_dn)
    return sq, gx, big, small, rs


def _place():
    x, y, c = lax.axis_index("x"), lax.axis_index("y"), lax.axis_index("c")
    return x, y, c


def _gather_blocks(x_refs, out_refs, send_sems, recv_sems, local_sems):
    n = len(out_refs)
    x, y, c = _place()
    me, sibling = (x, y, c), (x, y, 1 - c)
    chips = [(1 - x, y), (x, 1 - y), (1 - x, 1 - y)]

    def slot(u, px, py, pc):
        return out_refs[u].at[4 * px + 2 * py + pc]

    def copy(u, k, block, to, src=None):
        return pltpu.make_async_remote_copy(
            src_ref=slot(u, *block) if src is None else src, dst_ref=slot(u, *block),
            send_sem=send_sems.at[u, k], recv_sem=recv_sems.at[u, k], device_id=to, device_id_type=MESH)

    mines = [pltpu.make_async_copy(x_refs[u], slot(u, *me), local_sems.at[u]) for u in range(n)]
    for cp in mines:
        cp.start()
    first = []
    for u in range(n):
        first.append(copy(u, 0, me, sibling, src=x_refs[u]))
        first += [copy(u, 1 + j, me, (*chip, c), src=x_refs[u]) for j, chip in enumerate(chips)]
    for cp in first:
        cp.start()
    passed = []
    for j, chip in enumerate(chips):
        for u in range(n):
            copy(u, 1 + j, (*chip, c), me).wait_recv()
            fwd = copy(u, 4 + j, (*chip, c), sibling)
            fwd.start()
            passed.append(fwd)
    for u in range(n):
        copy(u, 0, sibling, me).wait_recv()
    for j, chip in enumerate(chips):
        for u in range(n):
            copy(u, 4 + j, (*chip, 1 - c), me).wait_recv()
    for cp in first + passed:
        cp.wait_send()
    for cp in mines:
        cp.wait()


def _gather_sems(n):
    return [pltpu.SemaphoreType.DMA((n, 7)), pltpu.SemaphoreType.DMA((n, 7)), pltpu.SemaphoreType.DMA((n,))]


def _cast_place(ws, c, dev):
    n = len(ws)

    def body(s_ref, *refs):
        for u in range(n):
            refs[n + u][0] = refs[u][...].astype(BF16)

    in_specs, out_specs, out_shape = [], [], []
    for w in ws:
        q, cols = w.shape[0] // 4, w.shape[1]
        in_specs.append(pl.BlockSpec((q, cols), lambda i, s: (2 * s[0] + i, 0)))
        out_specs.append(pl.BlockSpec((1, q, cols), lambda i, s: (s[1], i, 0)))
        out_shape.append(jax.ShapeDtypeStruct((8, 2 * q, cols), BF16))
    return pl.pallas_call(
        body, name="cast_place",
        grid_spec=pltpu.PrefetchScalarGridSpec(num_scalar_prefetch=1, grid=(2,), in_specs=in_specs,
                                               out_specs=out_specs),
        out_shape=_out_hbm(out_shape),
        compiler_params=pltpu.CompilerParams(vmem_limit_bytes=48 << 20))(jnp.stack([c, dev]), *_in_hbm(ws))


def _gather_phases(out_refs, send_sems, recv_sems, rows=None):
    n = len(out_refs)
    x, y, c = _place()
    me, sibling = (x, y, c), (x, y, 1 - c)
    chips = [(1 - x, y), (x, 1 - y), (1 - x, 1 - y)]

    def copy(u, k, block, to):
        px, py, pc = block
        ref = out_refs[u].at[4 * px + 2 * py + pc]
        if rows is not None and rows[u] is not None:
            ref = ref.at[pl.ds(rows[u][0], rows[u][1])]
        return pltpu.make_async_remote_copy(src_ref=ref, dst_ref=ref, send_sem=send_sems.at[u, k],
                                            recv_sem=recv_sems.at[u, k], device_id=to, device_id_type=MESH)

    def start():
        for u in range(n):
            copy(u, 0, me, sibling).start()
            for j, chip in enumerate(chips):
                copy(u, 1 + j, me, (*chip, c)).start()

    def mid():
        for j, chip in enumerate(chips):
            for u in range(n):
                copy(u, 1 + j, (*chip, c), me).wait_recv()
                copy(u, 4 + j, (*chip, c), sibling).start()

    def end():
        for u in range(n):
            copy(u, 0, sibling, me).wait_recv()
        for j, chip in enumerate(chips):
            for u in range(n):
                copy(u, 4 + j, (*chip, 1 - c), me).wait_recv()
        for u in range(n):
            copy(u, 0, me, sibling).wait_send()
            for j, chip in enumerate(chips):
                copy(u, 1 + j, me, (*chip, c)).wait_send()
                copy(u, 4 + j, (*chip, c), sibling).wait_send()

    return start, mid, end


def _carry_gather(bufs, rows=None):
    n = len(bufs)
    return _Carry(bufs, [jax.ShapeDtypeStruct(b.shape, b.dtype) for b in bufs], {u: u for u in range(n)},
                  [pltpu.SemaphoreType.DMA((n, 7)), pltpu.SemaphoreType.DMA((n, 7))],
                  lambda ins, outs, sems: _gather_phases(outs, *sems, rows=rows), "both")


def _ag_small(raw, sq):
    def body(dss1, dg1, dss2, dg2, dnw1, dnw2, dhw4, dqnw8, dknw2, dsk, dlg0, dlg1, sq_ref,
             out_ref, tot_ref, blk, send_sems, recv_sems, local_sems):
        _peer_barrier("both")
        blk[...] = jnp.zeros_like(blk)
        blk[0:2, :] = dss1[1]
        blk[2:3, :] = dg1[...]
        blk[3:5, :] = dss2[...]
        blk[5:6, :] = dg2[...]
        blk[6:8, :] = dss1[0]
        blk[8:9, :] = dnw1[...]
        blk[9:10, :] = dnw2[...]
        blk[10:11, 0:HGW] = dhw4[...]
        blk[10:11, HGW:D] = dqnw8[...]
        blk[11:12, 0:128] = dknw2[...]
        blk[12:14, 0:HGW] = dlg0[0]
        blk[14:16, 0:HGW] = dlg1[0]
        blk[16:24, 0:128] = dsk[...]
        blk[24:25, :] = sq_ref[...]
        _gather_blocks([blk], [out_ref], send_sems, recv_sems, local_sems)
        acc = out_ref[0]
        for i in range(1, 8):
            acc = acc + out_ref[i]
        tot_ref[...] = acc

    vm = pl.BlockSpec(memory_space=pltpu.VMEM)
    return pl.pallas_call(
        body, name="ag_small",
        out_shape=[jax.ShapeDtypeStruct((8, 32, D), F32), jax.ShapeDtypeStruct((32, D), F32)],
        in_specs=[vm] * 13, out_specs=[vm, vm],
        scratch_shapes=[pltpu.VMEM((32, D), F32)] + _gather_sems(1),
        compiler_params=pltpu.CompilerParams(collective_id=BARRIER_IDS["both"]))(*raw, sq)


def _pairx_shapes(units):
    return [jax.ShapeDtypeStruct((4,) + g.shape[2:], g.dtype) for g in units]


def _pairx_phases(g_refs, r_refs, send_sems, recv_sems):
    n = len(g_refs)
    x, y, c = _place()
    cps = [pltpu.make_async_remote_copy(
        src_ref=g_refs[u].at[j, 1 - c], dst_ref=r_refs[u].at[j], send_sem=send_sems.at[u, j],
        recv_sem=recv_sems.at[u, j], device_id=(x, y, 1 - c), device_id_type=MESH)
        for u in range(n) for j in range(4)]

    def start():
        for cp in cps:
            cp.start()

    def end():
        for cp in cps:
            cp.wait()

    return start, None, end


def _carry_pairx(units):
    n = len(units)
    return _Carry(units, _pairx_shapes(units), {},
                  [pltpu.SemaphoreType.DMA((n, 4)), pltpu.SemaphoreType.DMA((n, 4))],
                  lambda ins, outs, sems: _pairx_phases(ins, outs, *sems), "sib")


def _pair_exchange(units):
    n = len(units)

    def body(*refs):
        _peer_barrier("sib")
        start, _, end = _pairx_phases(refs[:n], refs[n:2 * n], *refs[2 * n:])
        start()
        end()

    return pl.pallas_call(
        body, name="pair_exchange", out_shape=_pairx_shapes(units), in_specs=[ANY] * n, out_specs=[ANY] * n,
        scratch_shapes=[pltpu.SemaphoreType.DMA((n, 4))] * 2,
        compiler_params=pltpu.CompilerParams(collective_id=BARRIER_IDS["sib"]))(*_in_hbm(units))


IN_ROWS_WITH_DH = 0.6


def _split_rows(h, share):
    first = int(h * share) // BF16_SUBLANES * BF16_SUBLANES
    return (0, first), (first, h - first)


def _rs_pair_add(units, recvs, c):
    n = len(units)

    def body(c_ref, *refs):
        for u in range(n):
            refs[2 * n + u][...] = (refs[u][0].astype(F32) + refs[n + u][...].astype(F32)).astype(BF16)

    in_specs, out_specs, out_shape = [], [], []
    for g in units:
        h, w = g.shape[2:]
        in_specs.append(pl.BlockSpec((1, 1, h, w), lambda j, cr: (j, cr[0], 0, 0)))
    for g in units:
        h, w = g.shape[2:]
        in_specs.append(pl.BlockSpec((1, h, w), lambda j, cr: (j, 0, 0)))
        out_specs.append(pl.BlockSpec((1, h, w), lambda j, cr: (j, 0, 0)))
        out_shape.append(jax.ShapeDtypeStruct((4, h, w), BF16))
    return pl.pallas_call(
        body, name="rs_pair_add",
        grid_spec=pltpu.PrefetchScalarGridSpec(num_scalar_prefetch=1, grid=(4,), in_specs=in_specs,
                                               out_specs=out_specs),
        out_shape=_out_hbm(out_shape),
        compiler_params=pltpu.CompilerParams(vmem_limit_bytes=48 << 20))(
            c.reshape(1), *_in_hbm(list(units) + list(recvs)))


def _chipx_phases(p_refs, r_refs, send_sems, recv_sems, rows=None):
    n = len(p_refs)
    x, y, c = _place()
    k = 2 * x + y

    def part(ref):
        return ref if rows is None else ref.at[pl.ds(rows[0], rows[1])]

    sends = []
    for d in range(1, 4):
        j = (k + d) % 4
        for u in range(n):
            sends.append(pltpu.make_async_remote_copy(
                src_ref=part(p_refs[u].at[j]), dst_ref=part(r_refs[u].at[k]), send_sem=send_sems.at[u, d - 1],
                recv_sem=recv_sems.at[u, d - 1], device_id=(j // 2, j % 2, c), device_id_type=MESH))

    def start():
        for cp in sends:
            cp.start()

    def end():
        for d in range(1, 4):
            src = (k + 4 - d) % 4
            for u in range(n):
                pltpu.make_async_remote_copy(
                    src_ref=part(p_refs[u].at[src]), dst_ref=part(r_refs[u].at[src]),
                    send_sem=send_sems.at[u, d - 1], recv_sem=recv_sems.at[u, d - 1], device_id=(x, y, c),
                    device_id_type=MESH).wait_recv()
        for cp in sends:
            cp.wait_send()

    return start, None, end


def _carry_chipx(pairs, rows=None, into=None):
    n = len(pairs)
    sems = [pltpu.SemaphoreType.DMA((n, 3)), pltpu.SemaphoreType.DMA((n, 3))]
    shapes = [jax.ShapeDtypeStruct(p.shape, p.dtype) for p in pairs]
    if into is None:
        return _Carry(pairs, shapes, {}, sems, lambda ins, outs, s: _chipx_phases(ins, outs, *s, rows=rows),
                      "chips")
    return _Carry(list(pairs) + list(into), shapes, {n + u: u for u in range(n)}, sems,
                  lambda ins, outs, s: _chipx_phases(ins[:n], outs, *s, rows=rows), "chips")


def _rs_chip_add(pairs, contribs, c, chip):
    n = len(pairs)

    def body(s_ref, *refs):
        for u in range(n):
            a, b, c_, d = refs[4 * u:4 * u + 4]
            refs[4 * n + u][0] = ((a[0].astype(F32) + b[0].astype(F32)) + c_[0].astype(F32)) + d[0].astype(F32)

    in_specs, out_specs, out_shape, args = [], [], [], []
    for p, r in zip(pairs, contribs):
        h, w = p.shape[1] // 2, p.shape[2]
        in_specs += [pl.BlockSpec((1, h, w), functools.partial(lambda d, i, s: ((s[1] + d) % 4, i, 0), d))
                     for d in range(4)]
        args += [p, r, r, r]
        out_specs.append(pl.BlockSpec((1, h, w), lambda i, s: (s[0], i, 0)))
        out_shape.append(jax.ShapeDtypeStruct((2, 2 * h, w), F32))
    return pl.pallas_call(
        body, name="rs_chip_add",
        grid_spec=pltpu.PrefetchScalarGridSpec(num_scalar_prefetch=1, grid=(2,), in_specs=in_specs,
                                               out_specs=out_specs),
        out_shape=_out_hbm(out_shape),
        compiler_params=pltpu.CompilerParams(vmem_limit_bytes=48 << 20))(jnp.stack([c, chip]), *_in_hbm(args))


def _rs_sibling_gather(reds, blks):
    n, k = len(reds), len(blks)
    vm = pl.BlockSpec(memory_space=pltpu.VMEM)

    def body(*refs):
        blk_in, red_out, blk_out = refs[n:n + k], refs[n + k:2 * n + k], refs[2 * n + k:2 * (n + k)]
        sems = refs[2 * (n + k):]
        _peer_barrier("both")
        start, _, end = _sibx_phases(red_out, *sems[:2])
        start()
        _gather_blocks(blk_in, blk_out, *sems[2:])
        end()

    res = pl.pallas_call(
        body, name="rs_sibling_gather",
        out_shape=[jax.ShapeDtypeStruct(r.shape, r.dtype) for r in reds]
        + [jax.ShapeDtypeStruct((8,) + b.shape, b.dtype) for b in blks],
        in_specs=[ANY] * n + [vm] * k, out_specs=[ANY] * n + [vm] * k,
        input_output_aliases={u: u for u in range(n)},
        scratch_shapes=[pltpu.SemaphoreType.DMA((n,))] * 2 + _gather_sems(k),
        compiler_params=pltpu.CompilerParams(collective_id=BARRIER_IDS["both"]))(*_in_hbm(reds), *blks)
    return res[:n], res[n:]


def _sibx_phases(o_refs, send_sems, recv_sems):
    n = len(o_refs)
    x, y, c = _place()
    cps = [pltpu.make_async_remote_copy(
        src_ref=o_refs[u].at[c], dst_ref=o_refs[u].at[c], send_sem=send_sems.at[u], recv_sem=recv_sems.at[u],
        device_id=(x, y, 1 - c), device_id_type=MESH) for u in range(n)]

    def start():
        for cp in cps:
            cp.start()

    def end():
        for u in range(n):
            cps[u].wait_send()
            pltpu.make_async_remote_copy(
                src_ref=o_refs[u].at[1 - c], dst_ref=o_refs[u].at[1 - c], send_sem=send_sems.at[u],
                recv_sem=recv_sems.at[u], device_id=(x, y, 1 - c), device_id_type=MESH).wait_recv()

    return start, None, end


def _carry_sibx(reds):
    n = len(reds)
    return _Carry(reds, [jax.ShapeDtypeStruct(r.shape, r.dtype) for r in reds], {u: u for u in range(n)},
                  [pltpu.SemaphoreType.DMA((n,))] * 2, lambda ins, outs, sems: _sibx_phases(outs, *sems), "sib")


def _prologue(blk, c_ctx, w, b, in8):
    n = w.shape[1]

    def body(blk_ref, cctx_ref, w_ref, b_ref, _in_in, g0_ref, c16_ref, g1_ref, in_ref, s1, r1, l1, s2, r2, s3, r3):
        _peer_barrier("both")
        x, y, c = _place()
        start, mid, end = _gather_phases([in_ref], s3, r3)
        start_mod, mid_mod, end_mod = _gather_phases([g1_ref], s2, r2)
        _gather_blocks([blk_ref], [g0_ref], s1, r1, l1)
        start()
        c16 = jnp.concatenate([g0_ref[i, 0:1, :] for i in range(8)] + [cctx_ref[...], jnp.zeros((7, D), F32)],
                              axis=0)
        c16_ref[...] = c16
        g1_ref[4 * x + 2 * y + c] = _dot(c16 * _sig(c16), w_ref[...], prec=HI) + b_ref[...]
        start_mod()
        mid()
        mid_mod()
        end()
        end_mod()

    vm = pl.BlockSpec(memory_space=pltpu.VMEM)
    return pl.pallas_call(
        body, name="prologue",
        out_shape=[jax.ShapeDtypeStruct((8, 8, D), F32), jax.ShapeDtypeStruct((16, D), F32),
                   jax.ShapeDtypeStruct((8, 16, n), F32), jax.ShapeDtypeStruct(in8.shape, in8.dtype)],
        in_specs=[vm, vm, vm, vm, ANY], out_specs=[vm, vm, vm, ANY], input_output_aliases={4: 3},
        scratch_shapes=_gather_sems(1) + [pltpu.SemaphoreType.DMA((1, 7))] * 4,
        compiler_params=pltpu.CompilerParams(vmem_limit_bytes=48 << 20,
                                             collective_id=BARRIER_IDS["both"]))(blk, c_ctx, w, b, in8)


def _ada_bwd(c16, dmod16, w, carry=None):
    n = w.shape[1]
    tn = 512

    def body(c_ref, d_ref, w_ref, gw_ref, gc_ref):
        j = pl.program_id(0)

        @pl.when(j == 0)
        def _():
            gc_ref[...] = jnp.zeros_like(gc_ref)

        cc = c_ref[...]
        dm = d_ref[...]
        gw_ref[...] = _dot(cc * _sig(cc), dm, TN, prec=HI)
        gc_ref[...] += _dot(dm, w_ref[...], NT, prec=HI)

    return _pcall(body, name="ada_bwd", grid=(n // tn,),
                  in_specs=[_full((16, D)), pl.BlockSpec((16, tn), lambda j: (0, j)),
                            pl.BlockSpec((D, tn), lambda j: (0, j))],
                  out_specs=[pl.BlockSpec((D, tn), lambda j: (0, j)), _full((16, D))],
                  out_shape=[jax.ShapeDtypeStruct((D, n), F32),
                             jax.ShapeDtypeStruct((16, D), F32)], carry=carry)(c16, dmod16, w)


def _adam_math(w, g, m, v):
    c1 = 1.0 - ADAM_B1 ** ADAM_STEP
    c2 = 1.0 - ADAM_B2 ** ADAM_STEP
    nm = ADAM_B1 * m + (1.0 - ADAM_B1) * g
    nv = ADAM_B2 * v + (1.0 - ADAM_B2) * (g * g)
    return -ADAM_LR * ((nm / c1) / (jnp.sqrt(nv / c2) + ADAM_EPS) + ADAM_WD * w), nm, nv


def _adamw_small(ws, gs, ms, vs):
    n = len(ws)

    def body(*refs):
        for u in range(n):
            d_, nm, nv = _adam_math(refs[u][...], refs[n + u][...], refs[2 * n + u][...], refs[3 * n + u][...])
            refs[4 * n + u][...] = d_
            refs[5 * n + u][...] = nm
            refs[6 * n + u][...] = nv

    specs = [_full(w.shape) for w in ws]
    shapes = [jax.ShapeDtypeStruct(w.shape, F32) for w in ws]
    out = _pcall(body, name="adamw_small", grid=(1,), in_specs=specs * 4, out_specs=specs * 3,
                 out_shape=shapes * 3)(*ws, *gs, *ms, *vs)
    return out[:n], out[n:2 * n], out[2 * n:]


def _cctx_grad(parts, c_ctx):
    def body(p_ref, c_ref, o_ref):
        acc = p_ref[0:1, :]
        for k in range(1, 4):
            acc = acc + p_ref[k:k + 1, :]
        cc = c_ref[...]
        s = _sig(cc)
        o_ref[...] = acc * (s * (1.0 + cc * (1.0 - s)))

    return _pcall(body, name="cctx_grad", grid=(1,), in_specs=[_full(parts.shape), _full((1, D))],
                  out_specs=_full((1, D)), out_shape=jax.ShapeDtypeStruct((1, D), F32))(parts, c_ctx)


ADAM_STEPS = 8


def _adamw_multi(ws, gs, ms, vs, *, name):
    n = len(ws)

    def body(*refs):
        for u in range(n):
            g = refs[n + u][...]
            refs[4 * n + u][...] = g
            refs[5 * n + u][...], refs[6 * n + u][...], refs[7 * n + u][...] = _adam_math(
                refs[u][...], g, refs[2 * n + u][...], refs[3 * n + u][...])

    specs = [pl.BlockSpec((w.shape[0] // ADAM_STEPS, w.shape[1]), lambda i: (i, 0)) for w in ws]
    shapes = [jax.ShapeDtypeStruct(w.shape, F32) for w in ws]
    out = _pcall(body, name=name, grid=(ADAM_STEPS,), in_specs=specs * 4, out_specs=specs * 4,
                 out_shape=shapes * 4)(*ws, *gs, *ms, *vs)
    return out[:n], out[n:2 * n], out[2 * n:3 * n], out[3 * n:]


def kernel(x, c, ctx, c_ctx, w_ada, b_ada, norm_mix_w, norm_ffn_w, w_in, hgrn_lb_logits, hgrn_norm_w, q_norm_w, k_norm_w, attn_sinks, w_branch_hgrn, w_branch_attn, w_out, w_ffn_gate, w_ffn_up, w_ffn_down, loss_target, m_c_ctx, m_w_ada, m_b_ada, m_norm_mix_w, m_norm_ffn_w, m_w_in, m_hgrn_lb_logits, m_hgrn_norm_w, m_q_norm_w, m_k_norm_w, m_attn_sinks, m_w_branch_hgrn, m_w_branch_attn, m_w_out, m_w_ffn_gate, m_w_ffn_up, m_w_ffn_down, v_c_ctx, v_w_ada, v_b_ada, v_norm_mix_w, v_norm_ffn_w, v_w_in, v_hgrn_lb_logits, v_hgrn_norm_w, v_q_norm_w, v_k_norm_w, v_attn_sinks, v_w_branch_hgrn, v_w_branch_attn, v_w_out, v_w_ffn_gate, v_w_ffn_up, v_w_ffn_down):
    xi, yi, ci = _place()
    chip = 2 * xi + yi
    dev = 2 * chip + ci
    s_len = x.shape[1]

    shards = [w_in[0].T, w_branch_hgrn[0], w_branch_attn[0], w_out[0], w_ffn_gate[0].T, w_ffn_up[0].T,
              w_ffn_down[0]]
    bufs = _cast_place(shards, ci, dev)

    lbrow = jnp.pad(hgrn_lb_logits.reshape(1, 512), ((0, 0), (0, D - 512)))
    blk = jnp.concatenate([c, lbrow, jnp.zeros((6, D), F32)], axis=0)
    nada = w_ada.shape[2]
    b_sh = lax.dynamic_slice(b_ada, (0, chip * nada), (1, nada))
    g0, c16, g1, in8 = _prologue(blk, c_ctx[None], w_ada[0], b_sh, bufs[0])
    lg = g0[0::2, 1, :512].reshape(4, 2, 2, 128).transpose(1, 2, 0, 3).reshape(2, 2, HGW)
    modall = g1[0::2].transpose(1, 0, 2).reshape(16, 4 * nada)
    mod = lax.dynamic_slice(modall, (dev, 0), (1, 6 * D)).reshape(6, D)
    modc = modall[8].reshape(6, D)[:2]

    sq, gx, _, small, rs = _local_step(
        x[0], ctx[0], loss_target[0], mod, modc, norm_mix_w, norm_ffn_w, lg, hgrn_norm_w, q_norm_w,
        k_norm_w, attn_sinks[0], in8.reshape(NCOL, D), bufs[1:], dist=(ci, chip))

    def whole(r):
        return r.reshape(2 * r.shape[1], r.shape[2])

    g_dn, g_g, g_u = [whole(r) for r in rs["ffn_done"]]
    g_bh, g_ba, g_o = [whole(r) for r in rs["mix_done"]]
    in_pairs = rs["in_pairs"]

    g2, tot = _ag_small(small["raw"], sq)
    loss = 0.5 * jnp.sum(tot[24]) / D
    dmodc_tot = jnp.pad(tot[6:8].reshape(1, 2 * D), ((0, 0), (0, 4 * D)))
    g_b_ada = tot[0:6].reshape(1, 6 * D) + dmodc_tot
    dmod16 = jnp.concatenate([g2[:, 0:6].reshape(8, 6 * D), dmodc_tot, jnp.zeros((7, 6 * D), F32)], axis=0)
    (g_w_ada, gc_part), in_contribs = _ada_bwd(
        c16, lax.dynamic_slice(dmod16, (0, chip * nada), (16, nada)), w_ada[0],
        carry=_carry_chipx(in_pairs, rows=rs["rest_rows"], into=rs["in_part"]))
    in_reds, (g3,) = _rs_sibling_gather(_rs_chip_add(in_pairs, in_contribs, ci, chip), [gc_part[8:16]])
    g_in = whole(in_reds[0])
    g_c_ctx = _cctx_grad(g3[0::2, 0], c_ctx[None])[0]
    g_nw1 = tot[8:9]
    g_nw2 = tot[9:10]
    g_hw = tot[10, :HGW].reshape(4, HGD).sum(0, keepdims=True)
    g_qnw = tot[10, HGW:].reshape(8, HDIM).sum(0, keepdims=True)
    g_knw = tot[11, :128].reshape(2, HDIM).sum(0, keepdims=True)
    g_sinks = tot[16:24, 0][None]
    g_lg = lax.dynamic_slice(tot[12:16, :HGW].reshape(2, 2, HGW), (0, 0, chip * 128), (2, 2, 128))

    names = ["c_ctx", "w_ada", "b_ada", "norm_mix_w", "norm_ffn_w", "w_in", "hgrn_lb_logits", "hgrn_norm_w",
             "q_norm_w", "k_norm_w", "attn_sinks", "w_branch_hgrn", "w_branch_attn", "w_out", "w_ffn_gate",
             "w_ffn_up", "w_ffn_down"]
    ws = dict(zip(names, [c_ctx, w_ada, b_ada, norm_mix_w, norm_ffn_w, w_in, hgrn_lb_logits, hgrn_norm_w,
                          q_norm_w, k_norm_w, attn_sinks, w_branch_hgrn, w_branch_attn, w_out, w_ffn_gate,
                          w_ffn_up, w_ffn_down]))
    ms = dict(zip(names, [m_c_ctx, m_w_ada, m_b_ada, m_norm_mix_w, m_norm_ffn_w, m_w_in, m_hgrn_lb_logits,
                          m_hgrn_norm_w, m_q_norm_w, m_k_norm_w, m_attn_sinks, m_w_branch_hgrn,
                          m_w_branch_attn, m_w_out, m_w_ffn_gate, m_w_ffn_up, m_w_ffn_down]))
    vs = dict(zip(names, [v_c_ctx, v_w_ada, v_b_ada, v_norm_mix_w, v_norm_ffn_w, v_w_in, v_hgrn_lb_logits,
                          v_hgrn_norm_w, v_q_norm_w, v_k_norm_w, v_attn_sinks, v_w_branch_hgrn,
                          v_w_branch_attn, v_w_out, v_w_ffn_gate, v_w_ffn_up, v_w_ffn_down]))
    transposed = ("w_in", "w_ffn_gate", "w_ffn_up")

    def view(a, n):
        return a[0].T if n in transposed else a[0]

    def unview(a, n):
        return a.T[None] if n in transposed else a[None]

    delta, new_m, new_v, grads = {}, {}, {}, {}

    def big_adamw(group, gs, name):
        g_, d_, m_, v_ = _adamw_multi([view(ws[n], n) for n in group], gs, [view(ms[n], n) for n in group],
                                      [view(vs[n], n) for n in group], name=name)
        for i, n in enumerate(group):
            grads[n], delta[n], new_m[n], new_v[n] = (unview(g_[i], n), unview(d_[i], n), unview(m_[i], n),
                                                      unview(v_[i], n))

    big_adamw(["w_ffn_down", "w_ffn_gate", "w_ffn_up", "w_out", "w_branch_hgrn", "w_branch_attn"],
              [g_dn, g_g, g_u, g_o, g_bh, g_ba], "adamw_first")
    big_adamw(["w_in", "w_ada"], [g_in, g_w_ada], "adamw_second")
    grads.update(c_ctx=g_c_ctx, b_ada=g_b_ada, norm_mix_w=g_nw1, norm_ffn_w=g_nw2, hgrn_lb_logits=g_lg,
                 hgrn_norm_w=g_hw, q_norm_w=g_qnw, k_norm_w=g_knw, attn_sinks=g_sinks)
    small_names = [n for n in names if n not in delta]

    def two_d(a):
        return a.reshape(1, -1) if a.ndim == 1 else a

    sd, sm_, sv = _adamw_small(*[[two_d(d[n]) for n in small_names] for d in (ws, grads, ms, vs)])
    for i, n in enumerate(small_names):
        for dst, src in ((delta, sd), (new_m, sm_), (new_v, sv)):
            dst[n] = src[i].reshape(ws[n].shape)
    return (loss, gx[None], *[grads[n] for n in names], *[delta[n] for n in names],
            *[new_m[n] for n in names], *[new_v[n] for n in names])
```

```python
import functools

import numpy as np
import jax
import jax.numpy as jnp
from jax import lax
from jax.experimental import pallas as pl
from jax.experimental.pallas import tpu as pltpu

F32 = jnp.float32
BF16 = jnp.bfloat16
HI = lax.Precision.HIGHEST
MESH = pl.DeviceIdType.MESH

D = 1024
L = 256
TM = 256
HGW = 512
HGD = 128
CH = 32
ATW = 512
HDIM = 64
BLK = 128
GRID_W = 64
DFF = 2816
NCOL = 5376
EPS = 1e-6
ROPE_THETA = 10000.0
BF16_SUBLANES = 16

C_FB, C_INP, C_QHG, C_FF = 0, 1, 2, 3
C_GATES = 1
C_GHG, C_QRAW = 8, 9
C_KV = 20
C_QKV = 6

ADAM_LR, ADAM_B1, ADAM_B2, ADAM_EPS, ADAM_WD, ADAM_STEP = 0.001, 0.9, 0.999, 1e-08, 0.01, 10

NN = (((1,), (0,)), ((), ()))
NT = (((1,), (1,)), ((), ()))
TN = (((0,), (0,)), ((), ()))


def _dot(a, b, dims=NN, prec=None):
    return lax.dot_general(a, b, dims, precision=prec, preferred_element_type=F32)


def _bdot(a, b, dims=NN):
    return _dot(a.astype(BF16), b.astype(BF16), dims)


def _sig(x):
    return 1.0 / (1.0 + jnp.exp(-x))


class _Carry:
    def __init__(self, ins, outs, aliases, scratch, phases, peers):
        self.ins, self.outs, self.aliases, self.scratch, self.phases = ins, outs, aliases, scratch, phases
        self.peers = peers


BARRIER_IDS = {"sib": 1, "chips": 2, "both": 3}


def _peer_barrier(kind):
    x, y, c = _place()
    peers = []
    if kind in ("sib", "both"):
        peers.append((x, y, 1 - c))
    if kind in ("chips", "both"):
        peers += [(1 - x, y, c), (x, 1 - y, c), (1 - x, 1 - y, c)]
    bar = pltpu.get_barrier_semaphore()
    for peer in peers:
        pl.semaphore_signal(bar, inc=1, device_id=peer, device_id_type=MESH)
    pl.semaphore_wait(bar, len(peers))


def _in_hbm(args):
    return [pltpu.with_memory_space_constraint(a, pltpu.HBM) for a in args]


def _out_hbm(shapes):
    if isinstance(shapes, (list, tuple)):
        return [pltpu.HBM(s.shape, s.dtype) for s in shapes]
    return pltpu.HBM(shapes.shape, shapes.dtype)


def _carry_join(a, b):
    na_in, na_out, na_sc = len(a.ins), len(a.outs), len(a.scratch)
    aliases = dict(a.aliases)
    aliases.update({na_in + i: na_out + o for i, o in b.aliases.items()})

    def phases(ins, outs, sems):
        pa = a.phases(ins[:na_in], outs[:na_out], sems[:na_sc])
        pb = b.phases(ins[na_in:], outs[na_out:], sems[na_sc:])

        def both(fa, fb):
            if fa is None and fb is None:
                return None

            def run():
                for fn in (fa, fb):
                    if fn is not None:
                        fn()
            return run

        return tuple(both(fa, fb) for fa, fb in zip(pa, pb))

    return _Carry(list(a.ins) + list(b.ins), list(a.outs) + list(b.outs), aliases,
                  list(a.scratch) + list(b.scratch), phases, a.peers if a.peers == b.peers else "both")


def _pcall(body, *, name, grid, in_specs, out_specs, out_shape, scratch=(), aliases=None, vmem_mb=48,
           carry=None):
    params = pltpu.CompilerParams(dimension_semantics=("arbitrary",) * len(grid),
                                  vmem_limit_bytes=vmem_mb << 20)
    if carry is None:
        plain = pl.pallas_call(
            body, name=name, grid=grid, in_specs=in_specs, out_specs=out_specs, out_shape=_out_hbm(out_shape),
            scratch_shapes=list(scratch), input_output_aliases=aliases or {}, compiler_params=params)
        return lambda *args: plain(*_in_hbm(args))
    single = not isinstance(out_shape, (list, tuple))
    out_specs_l = [out_specs] if single else list(out_specs)
    out_shape_l = [out_shape] if single else list(out_shape)
    n_in, n_out, n_sc = len(in_specs), len(out_shape_l), len(scratch)
    k_in, k_out = len(carry.ins), len(carry.outs)
    nsteps = int(np.prod(grid))
    assert nsteps >= 3

    def wrapped(*refs):
        ins, cins = refs[:n_in], refs[n_in:n_in + k_in]
        o0 = n_in + k_in
        outs, couts = refs[o0:o0 + n_out], refs[o0 + n_out:o0 + n_out + k_out]
        s0 = o0 + n_out + k_out
        sc, csc = refs[s0:s0 + n_sc], refs[s0 + n_sc:]
        step = pl.program_id(0)
        for ax in range(1, len(grid)):
            step = step * grid[ax] + pl.program_id(ax)
        start, mid, end = carry.phases(cins, couts, csc)

        @pl.when(step == 0)
        def _():
            _peer_barrier(carry.peers)
            start()

        body(*ins, *outs, *sc)
        if mid is not None:
            pl.when(step == nsteps - 2)(mid)
        pl.when(step == nsteps - 1)(end)

    all_aliases = dict(aliases or {})
    all_aliases.update({n_in + i: n_out + o for i, o in carry.aliases.items()})
    call = pl.pallas_call(
        wrapped, name=name, grid=grid, in_specs=list(in_specs) + [ANY] * k_in,
        out_specs=out_specs_l + [ANY] * k_out, out_shape=_out_hbm(out_shape_l + list(carry.outs)),
        scratch_shapes=list(scratch) + list(carry.scratch), input_output_aliases=all_aliases,
        compiler_params=pltpu.CompilerParams(dimension_semantics=("arbitrary",) * len(grid),
                                             vmem_limit_bytes=vmem_mb << 20,
                                             collective_id=BARRIER_IDS[carry.peers]))

    def run(*args):
        res = call(*_in_hbm(args), *_in_hbm(carry.ins))
        core = res[:n_out]
        return (core[0] if single else list(core)), list(res[n_out:])

    return run


def _full(shape):
    nd = len(shape)
    return pl.BlockSpec(shape, lambda *_: (0,) * nd)


ANY = pl.BlockSpec(memory_space=pl.ANY)


NT_IN = NCOL // 256


def _src_block(j):
    return j + jnp.where(j < 4, 2, jnp.where(j < 6, 3, jnp.where(j < 8, -6, jnp.where(
        j < 16, 5, jnp.where(j < 20, -7, -14)))))


def _mm_in(h, wt, tm, carry=None):
    tt = h.shape[0]

    def body(h_ref, w_ref, o_ref):
        o_ref[...] = _bdot(h_ref[...], w_ref[...], NT)

    return _pcall(body, name="mm_in", grid=(tt // tm, NT_IN),
                  in_specs=[pl.BlockSpec((tm, D), lambda i, j: (i, 0)),
                            pl.BlockSpec((256, D), lambda i, j: (_src_block(j), 0))],
                  out_specs=pl.BlockSpec((tm, 256), lambda i, j: (i, j)),
                  out_shape=jax.ShapeDtypeStruct((tt, NCOL), F32), carry=carry)(h, wt)


def _mm_dh(dp, wt, tm, carry=None):
    tt = dp.shape[0]
    per, ng = 3, NT_IN // 3

    def body(d_ref, w0, w1, w2, o_ref, acc):
        kk = pl.program_id(1)

        @pl.when(kk == 0)
        def _():
            acc[...] = jnp.zeros_like(acc)

        acc[...] += (_bdot(d_ref[:, 0:256], w0[...]) + _bdot(d_ref[:, 256:512], w1[...])
                     + _bdot(d_ref[:, 512:768], w2[...]))

        @pl.when(kk == ng - 1)
        def _():
            o_ref[...] = acc[...]

    wspecs = [pl.BlockSpec((256, D), functools.partial(lambda t, i, kk: (_src_block(per * kk + t), 0), t))
              for t in range(per)]
    return _pcall(body, name="mm_dh", grid=(tt // tm, ng),
                  in_specs=[pl.BlockSpec((tm, per * 256), lambda i, kk: (i, kk))] + wspecs,
                  out_specs=pl.BlockSpec((tm, D), lambda i, kk: (i, 0)),
                  out_shape=jax.ShapeDtypeStruct((tt, D), F32), scratch=[pltpu.VMEM((tm, D), F32)],
                  carry=carry)(dp, wt, wt, wt)


def _mm_gin(dp, h, tk):
    tt = dp.shape[0]
    nk = tt // tk

    def body(d_ref, h_ref, o_ref, acc):
        kk = pl.program_id(1)

        @pl.when(kk == 0)
        def _():
            acc[...] = jnp.zeros_like(acc)

        acc[...] += _bdot(d_ref[...], h_ref[...], TN)

        @pl.when(kk == nk - 1)
        def _():
            o_ref[...] = acc[...].astype(BF16)

    return _pcall(body, name="mm_gin", grid=(NT_IN, nk),
                  in_specs=[pl.BlockSpec((tk, 256), lambda j, kk: (kk, j)),
                            pl.BlockSpec((tk, D), lambda j, kk: (kk, 0))],
                  out_specs=pl.BlockSpec((256, D), lambda j, kk: (_src_block(j), 0)),
                  out_shape=jax.ShapeDtypeStruct((NCOL, D), BF16), scratch=[pltpu.VMEM((256, D), F32)])(dp, h)


def _tok_specs():
    assert L == TM
    return [_full((TM, D)), pl.BlockSpec((TM, D), lambda i: (jnp.maximum(i - 1, 0), 0))]


def _mod1(ctx, x, nw, ss):
    rows = L + x.shape[0]

    def body(c_ref, x_ref, nw_ref, ss_ref, h_ref):
        t = jnp.where(pl.program_id(0) == 0, c_ref[...], x_ref[...])
        r = lax.rsqrt(jnp.mean(t * t, axis=-1, keepdims=True) + EPS)
        s = ss_ref[0]
        h_ref[...] = ((t * r * nw_ref[...]) * (1.0 + s[1:2]) + s[0:1]).astype(BF16)

    return _pcall(body, name="mod1", grid=(rows // TM,),
                  in_specs=_tok_specs() + [_full((1, D)),
                                           pl.BlockSpec((1, 2, D), lambda i: (jnp.minimum(i, 1), 0, 0))],
                  out_specs=pl.BlockSpec((TM, D), lambda i: (i, 0)),
                  out_shape=jax.ShapeDtypeStruct((rows, D), BF16))(ctx, x, nw, ss)


def _norm_bwd_rows(x, dh, nw, scale):
    r = lax.rsqrt(jnp.mean(x * x, axis=-1, keepdims=True) + EPS)
    xh = x * r
    dxh = dh * ((1.0 + scale) * nw)
    dx = r * (dxh - xh * jnp.mean(dxh * xh, axis=-1, keepdims=True))
    return dx, xh


def _out_proj_mod2(mixed, w_o, x, g1, nw2, ss2):
    s_len = x.shape[0]
    tm = 512

    def body(m_ref, w_ref, x_ref, g_ref, nw_ref, ss_ref, ao_ref, x1_ref, h_ref):
        ao = _bdot(m_ref[...], w_ref[...])
        ao_ref[...] = ao.astype(BF16)
        x1 = x_ref[...] + g_ref[...] * ao
        x1_ref[...] = x1
        r = lax.rsqrt(jnp.mean(x1 * x1, axis=-1, keepdims=True) + EPS)
        s = ss_ref[0]
        h_ref[...] = ((x1 * r * nw_ref[...]) * (1.0 + s[1:2]) + s[0:1]).astype(BF16)

    row = pl.BlockSpec((tm, D), lambda i: (i, 0))
    f = jax.ShapeDtypeStruct((s_len, D), F32)
    return _pcall(body, name="out_proj_mod2", grid=(s_len // tm,),
                  in_specs=[row, _full((D, D)), row, _full((1, D)), _full((1, D)), _full((1, 2, D))],
                  out_specs=[row, row, row],
                  out_shape=[jax.ShapeDtypeStruct((s_len, D), BF16), f,
                             jax.ShapeDtypeStruct((s_len, D), BF16)])(mixed, w_o, x, g1, nw2, ss2)


TS = 1024


def _acc_call(body, *, name, grid, in_specs, out_specs, out_shape, acc_shapes, args, carry=None):
    return _pcall(body, name=name, grid=grid, in_specs=in_specs, out_specs=out_specs, out_shape=out_shape,
                  scratch=[pltpu.VMEM(s, F32) for s in acc_shapes], carry=carry)(*args)


def _ffn_up(h2, g4, u4, carry=None):
    s_len = h2.shape[0]
    ns = g4.shape[1]

    def body(h_ref, g_ref, u_ref, a_ref, b_ref, z_ref):
        h = h_ref[...]
        a = _bdot(h, g_ref[0], NT)
        b = _bdot(h, u_ref[0], NT)
        a_ref[0] = a.astype(BF16)
        b_ref[0] = b.astype(BF16)
        z_ref[0] = (a * _sig(a) * b).astype(BF16)

    w = pl.BlockSpec((1, ns, D), lambda i, j: (j, 0, 0))
    o = pl.BlockSpec((1, TS, ns), lambda i, j: (j, i, 0))
    f = jax.ShapeDtypeStruct((4, s_len, ns), BF16)
    return _pcall(body, name="ffn_up", grid=(s_len // TS, 4),
                  in_specs=[pl.BlockSpec((TS, D), lambda i, j: (i, 0)), w, w], out_specs=[o, o, o],
                  out_shape=[f, f, jax.ShapeDtypeStruct((4, s_len, ns), BF16)], carry=carry)(h2, g4, u4)


def _ffn_down_loss(z4, dn4, x1, g2, tgt):
    _, s_len, ns = z4.shape

    def body(z_ref, w_ref, x1_ref, g_ref, t_ref, sq_ref, dx2_ref, dyb_ref, dg_ref, acc):
        i, j = pl.program_id(0), pl.program_id(1)

        @pl.when((i == 0) & (j == 0))
        def _():
            sq_ref[...] = jnp.zeros_like(sq_ref)
            dg_ref[...] = jnp.zeros_like(dg_ref)

        @pl.when(j == 0)
        def _():
            acc[...] = jnp.zeros_like(acc)

        acc[...] += _bdot(z_ref[0], w_ref[0])

        @pl.when(j == 3)
        def _():
            y_ = acc[...]
            g = g_ref[...]
            e = x1_ref[...] + g * y_ - t_ref[...]
            sq_ref[...] += jnp.sum(e * e, axis=0, keepdims=True)
            dx2 = e * (1.0 / D)
            dx2_ref[...] = dx2
            dyb_ref[...] = (g * dx2).astype(BF16)
            dg_ref[...] += jnp.sum(dx2 * y_, axis=0, keepdims=True)

    row = pl.BlockSpec((TS, D), lambda i, j: (i, 0))
    vec = _full((1, D))
    return _acc_call(body, name="ffn_down_loss", grid=(s_len // TS, 4),
                     in_specs=[pl.BlockSpec((1, TS, ns), lambda i, j: (j, i, 0)),
                               pl.BlockSpec((1, ns, D), lambda i, j: (j, 0, 0)), row, vec, row],
                     out_specs=[vec, row, row, vec],
                     out_shape=[jax.ShapeDtypeStruct((1, D), F32), jax.ShapeDtypeStruct((s_len, D), F32),
                                jax.ShapeDtypeStruct((s_len, D), BF16), jax.ShapeDtypeStruct((1, D), F32)],
                     acc_shapes=[(TS, D)], args=(z4, dn4, x1, g2, tgt))


def _ffn_dz(dyb, dn4, a4, b4):
    _, s_len, ns = a4.shape

    def body(dy_ref, w_ref, a_ref, b_ref, da_ref, db_ref):
        dz = _bdot(dy_ref[...], w_ref[0], NT)
        a = a_ref[0].astype(F32)
        s = _sig(a)
        da_ref[0] = (dz * b_ref[0].astype(F32) * (s * (1.0 + a * (1.0 - s)))).astype(BF16)
        db_ref[0] = (dz * (a * s)).astype(BF16)

    t = pl.BlockSpec((1, TS, ns), lambda i, j: (j, i, 0))
    o = jax.ShapeDtypeStruct((4, s_len, ns), BF16)
    return _pcall(body, name="ffn_dz", grid=(s_len // TS, 4),
                  in_specs=[pl.BlockSpec((TS, D), lambda i, j: (i, 0)),
                            pl.BlockSpec((1, ns, D), lambda i, j: (j, 0, 0)), t, t],
                  out_specs=[t, t], out_shape=[o, o])(dyb, dn4, a4, b4)


def _ffn_gdn(z4, dyb):
    _, s_len, ns = z4.shape
    tk = min(s_len, 2 * TS)
    nk = s_len // tk

    def body(z_ref, dy_ref, o_ref, acc):
        t = pl.program_id(1)

        @pl.when(t == 0)
        def _():
            acc[...] = jnp.zeros_like(acc)

        acc[...] += _bdot(z_ref[0], dy_ref[...], TN)

        @pl.when(t == nk - 1)
        def _():
            o_ref[0] = acc[...].astype(o_ref.dtype)

    return _acc_call(body, name="ffn_gdn", grid=(4, nk),
                     in_specs=[pl.BlockSpec((1, tk, ns), lambda j, t: (j, t, 0)),
                               pl.BlockSpec((tk, D), lambda j, t: (t, 0))],
                     out_specs=pl.BlockSpec((1, ns, D), lambda j, t: (j, 0, 0)),
                     out_shape=jax.ShapeDtypeStruct((4, ns, D), BF16), acc_shapes=[(ns, D)], args=(z4, dyb))


def _ffn_dh2(da4, db4, g4, u4, carry=None):
    _, s_len, ns = da4.shape

    def body(da_ref, db_ref, g_ref, u_ref, o_ref, acc):
        j = pl.program_id(1)

        @pl.when(j == 0)
        def _():
            acc[...] = jnp.zeros_like(acc)

        acc[...] += _bdot(da_ref[0], g_ref[0]) + _bdot(db_ref[0], u_ref[0])

        @pl.when(j == 3)
        def _():
            o_ref[...] = acc[...]

    t = pl.BlockSpec((1, TS, ns), lambda i, j: (j, i, 0))
    w = pl.BlockSpec((1, ns, D), lambda i, j: (j, 0, 0))
    return _acc_call(body, name="ffn_dh2", grid=(s_len // TS, 4), in_specs=[t, t, w, w],
                     out_specs=pl.BlockSpec((TS, D), lambda i, j: (i, 0)),
                     out_shape=jax.ShapeDtypeStruct((s_len, D), F32), acc_shapes=[(TS, D)],
                     args=(da4, db4, g4, u4), carry=carry)


def _ffn_ggu(h2, da4, db4, carry=None):
    _, s_len, ns = da4.shape
    nk = s_len // TS

    def body(h_ref, da_ref, db_ref, gg_ref, gu_ref, acc_g, acc_u):
        t = pl.program_id(1)

        @pl.when(t == 0)
        def _():
            acc_g[...] = jnp.zeros_like(acc_g)
            acc_u[...] = jnp.zeros_like(acc_u)

        h = h_ref[...]
        acc_g[...] += _bdot(da_ref[0], h, TN)
        acc_u[...] += _bdot(db_ref[0], h, TN)

        @pl.when(t == nk - 1)
        def _():
            gg_ref[0] = acc_g[...].astype(BF16)
            gu_ref[0] = acc_u[...].astype(BF16)

    d = pl.BlockSpec((1, TS, ns), lambda j, t: (j, t, 0))
    o = pl.BlockSpec((1, ns, D), lambda j, t: (j, 0, 0))
    f = jax.ShapeDtypeStruct((4, ns, D), BF16)
    return _acc_call(body, name="ffn_ggu", grid=(4, nk),
                     in_specs=[pl.BlockSpec((TS, D), lambda j, t: (t, 0)), d, d], out_specs=[o, o],
                     out_shape=[f, f], acc_shapes=[(ns, D), (ns, D)], args=(h2, da4, db4), carry=carry)


def _mod2_bwd(x1, dh2, dx2, ao, nw2, ss2, g1):
    s_len = x1.shape[0]

    def body(x1_ref, dh_ref, dx2_ref, ao_ref, nw_ref, ss_ref, g_ref,
             dx1_ref, da_ref, dss_ref, dnw_ref, dg_ref):
        i = pl.program_id(0)

        @pl.when(i == 0)
        def _():
            dss_ref[...] = jnp.zeros_like(dss_ref)
            dnw_ref[...] = jnp.zeros_like(dnw_ref)
            dg_ref[...] = jnp.zeros_like(dg_ref)

        dh = dh_ref[...]
        nw = nw_ref[...]
        scale = ss_ref[0][1:2]
        dxn, xh = _norm_bwd_rows(x1_ref[...], dh, nw, scale)
        dx1 = dx2_ref[...] + dxn
        dx1_ref[...] = dx1
        da_ref[...] = (g_ref[...] * dx1).astype(BF16)
        dg_ref[...] += jnp.sum(dx1 * ao_ref[...].astype(F32), axis=0, keepdims=True)
        dsh = jnp.sum(dh, axis=0, keepdims=True)
        dsc = jnp.sum(dh * xh * nw, axis=0, keepdims=True)
        dss_ref[...] += jnp.concatenate([dsh, dsc], axis=0)
        dnw_ref[...] += jnp.sum(dh * xh * (1.0 + scale), axis=0, keepdims=True)

    row = pl.BlockSpec((TM, D), lambda i: (i, 0))
    vec = _full((1, D))
    return _pcall(body, name="mod2_bwd", grid=(s_len // TM,),
                  in_specs=[row, row, row, row, vec, _full((1, 2, D)), vec],
                  out_specs=[row, row, _full((2, D)), vec, vec],
                  out_shape=[jax.ShapeDtypeStruct((s_len, D), F32), jax.ShapeDtypeStruct((s_len, D), BF16),
                             jax.ShapeDtypeStruct((2, D), F32), jax.ShapeDtypeStruct((1, D), F32),
                             jax.ShapeDtypeStruct((1, D), F32)])(x1, dh2, dx2, ao, nw2, ss2, g1)


def _mod1_bwd(ctx, x, dh, dx1, nw1, ss1):
    s_len = dx1.shape[0]
    tt = L + s_len

    def body(c_ref, x_ref, dh_ref, dx1_ref, nw_ref, ss_ref, dx_ref, dss_ref, dnw_ref):
        i = pl.program_id(0)
        tok = jnp.where(i == 0, c_ref[...], x_ref[...])

        @pl.when(i == 0)
        def _():
            dnw_ref[...] = jnp.zeros_like(dnw_ref)

        @pl.when(i <= 1)
        def _():
            dss_ref[...] = jnp.zeros_like(dss_ref)

        dh_ = dh_ref[...]
        nw = nw_ref[...]
        scale = ss_ref[0][1:2]
        dxn, xh = _norm_bwd_rows(tok, dh_, nw, scale)

        @pl.when(i >= 1)
        def _():
            dx_ref[...] = dx1_ref[...] + dxn

        dsh = jnp.sum(dh_, axis=0, keepdims=True)
        dsc = jnp.sum(dh_ * xh * nw, axis=0, keepdims=True)
        dss_ref[...] += jnp.concatenate([dsh, dsc], axis=0)[None]
        dnw_ref[...] += jnp.sum(dh_ * xh * (1.0 + scale), axis=0, keepdims=True)

    row = pl.BlockSpec((TM, D), lambda i: (i, 0))
    lat = pl.BlockSpec((TM, D), lambda i: (jnp.maximum(i - 1, 0), 0))
    sel = pl.BlockSpec((1, 2, D), lambda i: (jnp.minimum(i, 1), 0, 0))
    return _pcall(body, name="mod1_bwd", grid=(tt // TM,),
                  in_specs=_tok_specs() + [row, lat, _full((1, D)), sel],
                  out_specs=[lat, sel, _full((1, D))],
                  out_shape=[jax.ShapeDtypeStruct((s_len, D), F32), jax.ShapeDtypeStruct((2, 2, D), F32),
                             jax.ShapeDtypeStruct((1, D), F32)])(ctx, x, dh, dx1, nw1, ss1)


def _rows(c):
    return slice(c * CH, (c + 1) * CH)


def _chunk_masks(rev, transpose=False):
    r = lax.broadcasted_iota(jnp.int32, (TM, TM), 0)
    c = lax.broadcasted_iota(jnp.int32, (TM, TM), 1)
    same = (r // CH) == (c // CH)
    before = (c >= r) if (rev != transpose) else (c <= r)
    return same & before, same


def _chunk_scan(x, rev, transpose=False):
    r = lax.broadcasted_iota(jnp.int32, (CH, CH), 0)
    c = lax.broadcasted_iota(jnp.int32, (CH, CH), 1)
    tri = ((c >= r) if (rev != transpose) else (c <= r)).astype(F32)
    return jnp.concatenate([_dot(tri, x[_rows(ch)], prec=HI) for ch in range(x.shape[0] // CH)], axis=0)


def _chunk_total(x):
    return jnp.concatenate([jnp.broadcast_to(jnp.sum(x[_rows(ch)], axis=0, keepdims=True), (CH, x.shape[1]))
                            for ch in range(x.shape[0] // CH)], axis=0)


def _hgrn_gate(fl, qraw, lg):
    lb = 1.0 / (1.0 + jnp.exp(lg[1:2] - lg[0:1]))
    sg = _sig(fl)
    f = lb + (1.0 - lb) * sg
    q = qraw * _sig(qraw) * (HGD ** -0.5)
    return lb, sg, f, q


def _hgrn_fwd(p, lg, *, rev, carry=None, readout=None):
    tt = p.shape[0]
    nt = tt // TM
    ncht = TM // CH
    d = 1 if rev else 0

    def tile_of(s):
        return jnp.where(s == 0, 0, nt - s) if rev else s

    def body(*refs):
        if readout is None:
            f_ref, inp_ref, q_ref, lg_ref, o_ref, st_ref, state = refs
        else:
            f_ref, inp_ref, q_ref, lg_ref, oo_ref, g_ref, hw_ref, o_ref, st_ref, y_ref, state = refs
        s = pl.program_id(0)

        @pl.when(s == 0)
        def _():
            state[...] = jnp.zeros_like(state)

        _, _, f, q = _hgrn_gate(f_ref[...], q_ref[...], lg_ref[0])
        lf = jnp.log(f)
        causal, _ = _chunk_masks(rev)
        cum = _chunk_scan(lf, rev)
        tot = _chunk_total(lf)
        qd = (q * jnp.exp(cum)).astype(BF16)
        kd = ((1.0 - f) * jnp.exp(-cum)).astype(BF16)
        ke = ((1.0 - f) * jnp.exp(tot - cum)).astype(BF16)
        et = jnp.exp(tot)
        v = inp_ref[...].astype(BF16)
        order = range(ncht - 1, -1, -1) if rev else range(ncht)
        outs = []
        for h in range(4):
            sl = slice(h * HGD, (h + 1) * HGD)
            qd_, kd_, ke_, v_ = qd[:, sl], kd[:, sl], ke[:, sl], v[:, sl]
            pm = jnp.where(causal, _dot(qd_, kd_, NT), 0.0).astype(BF16)
            o_h = _dot(pm, v_)
            upd = [_dot(v_[_rows(c)], ke_[_rows(c)], TN) for c in range(ncht)]
            st = state[h]
            for c in order:
                st_ref[c, h] = st
                st = st * et[c * CH:c * CH + 1, sl] + upd[c]
            state[h] = st
            inter = [_dot(qd_[_rows(c)], st_ref[c, h].astype(BF16), NT) for c in range(ncht)]
            outs.append(o_h + jnp.concatenate(inter, axis=0))
        o_tile = jnp.concatenate(outs, axis=1)
        o_ref[...] = o_tile
        if readout is not None:
            @pl.when(tile_of(s) >= 1)
            def _():
                g = g_ref[...]
                y_ref[...] = (_head_rms(oo_ref[...] + o_tile, None, 4) * hw_ref[...] * (g * _sig(g))).astype(BF16)

    def col(cb):
        return pl.BlockSpec((TM, HGW), lambda s: (tile_of(s), cb))

    in_specs = [col(C_FB if rev else C_FF), col(C_INP), col(C_QHG), pl.BlockSpec((1, 2, HGW), lambda s: (d, 0, 0))]
    out_specs = [col(0), pl.BlockSpec((ncht, 4, HGD, HGD), lambda s: (tile_of(s), 0, 0, 0))]
    out_shape = [jax.ShapeDtypeStruct((tt, HGW), F32), jax.ShapeDtypeStruct((nt * ncht, 4, HGD, HGD), F32)]
    args = [p, p, p, lg]
    if readout is not None:
        in_specs += [col(0), col(C_GHG), _full((1, HGW))]
        args += [readout[0], p, readout[1]]
        assert rev
        out_specs.append(pl.BlockSpec((TM, HGW), lambda s: (jnp.where(s == 0, nt - 2, tile_of(s) - 1), 0)))
        out_shape.append(jax.ShapeDtypeStruct((tt - L, HGW), BF16))
    return _pcall(body, name="hgrn_fwd_rev" if rev else "hgrn_fwd", grid=(nt,), in_specs=in_specs,
                  out_specs=out_specs, out_shape=out_shape, scratch=[pltpu.VMEM((4, HGD, HGD), F32)],
                  carry=carry)(*args)


def _hgrn_bwd(p, lg, do, st, dp, prev, *, rev, carry=None):
    tt = p.shape[0]
    nt = tt // TM
    ncht = TM // CH
    d = 1 if rev else 0
    second = prev is not None

    def tile_of(s):
        return jnp.where(s == nt - 1, 0, s + 1) if rev else nt - 1 - s

    def body(*refs):
        if second:
            (f_ref, inp_ref, q_ref, lg_ref, do_ref, st_ref, dvp_ref, dqp_ref, _dp_in,
             dp_ref, dlg_ref, dstate) = refs
        else:
            (f_ref, inp_ref, q_ref, lg_ref, do_ref, st_ref, _dp_in,
             dp_ref, dv_ref, dq_ref, dlg_ref, dstate) = refs
        s = pl.program_id(0)
        tile = tile_of(s)

        @pl.when(s == 0)
        def _():
            dstate[...] = jnp.zeros_like(dstate)
            dlg_ref[...] = jnp.zeros_like(dlg_ref)

        qraw = q_ref[...]
        lb, sg, f, q = _hgrn_gate(f_ref[...], qraw, lg_ref[0])
        lf = jnp.log(f)
        causal, _ = _chunk_masks(rev)
        causal_t, _ = _chunk_masks(rev, transpose=True)
        cum = _chunk_scan(lf, rev)
        tot = _chunk_total(lf)
        ea, eb, ee, et = jnp.exp(cum), jnp.exp(-cum), jnp.exp(tot - cum), jnp.exp(tot)
        qdf, kdf, kef = q * ea, (1.0 - f) * eb, (1.0 - f) * ee
        qd, kd, ke = qdf.astype(BF16), kdf.astype(BF16), kef.astype(BF16)
        v = inp_ref[...].astype(BF16)
        dob = jnp.where(tile == 0, 0.0, do_ref[...]).astype(BF16)
        order = range(ncht) if rev else range(ncht - 1, -1, -1)
        dq_l, dk_l, dv_l, dcum_l, dtot_l = [], [], [], [], []
        for h in range(4):
            sl = slice(h * HGD, (h + 1) * HGD)
            qd_, kd_, ke_, v_, do_ = qd[:, sl], kd[:, sl], ke[:, sl], v[:, sl], dob[:, sl]
            pmt = jnp.where(causal_t, _dot(kd_, qd_, NT), 0.0).astype(BF16)
            dpm = jnp.where(causal, _dot(do_, v_, NT), 0.0).astype(BF16)
            dpmt = jnp.where(causal_t, _dot(v_, do_, NT), 0.0).astype(BF16)
            dv = _dot(pmt, do_)
            dqd = _dot(dpm, kd_)
            dkd = _dot(dpmt, qd_)
            upd = [_dot(do_[_rows(c)], qd_[_rows(c)], TN) for c in range(ncht)]
            ds = dstate[h]
            ds1 = [None] * ncht
            for c in order:
                ds1[c] = ds
                ds = ds * et[c * CH:c * CH + 1, sl] + upd[c]
            dstate[h] = ds
            dke_c, dv_c, dqd_c, dtot_c = [], [], [], []
            for c in range(ncht):
                st0 = st_ref[c, h]
                dsb = ds1[c].astype(BF16)
                dke_ = _dot(v_[_rows(c)], dsb)
                dke_c.append(dke_)
                dv_c.append(_dot(ke_[_rows(c)], dsb, NT))
                dqd_c.append(_dot(do_[_rows(c)], st0.astype(BF16)))
                dt = (jnp.sum(ds1[c] * st0, axis=0, keepdims=True) * et[c * CH:c * CH + 1, sl]
                      + jnp.sum(dke_ * kef[_rows(c), sl], axis=0, keepdims=True))
                dtot_c.append(jnp.broadcast_to(dt, (CH, HGD)))
            dke = jnp.concatenate(dke_c, axis=0)
            dqd = dqd + jnp.concatenate(dqd_c, axis=0)
            dv_l.append(dv + jnp.concatenate(dv_c, axis=0))
            dtot_l.append(jnp.concatenate(dtot_c, axis=0))
            dq_l.append(dqd * ea[:, sl])
            dk_l.append(dkd * eb[:, sl] + dke * ee[:, sl])
            dcum_l.append(dqd * qdf[:, sl] - dkd * kdf[:, sl] - dke * kef[:, sl])
        dcum = jnp.concatenate(dcum_l, axis=1)
        dlf = _chunk_scan(dcum, rev, transpose=True) + jnp.concatenate(dtot_l, axis=1)
        dq_t = jnp.concatenate(dq_l, axis=1)
        dv_t = jnp.concatenate(dv_l, axis=1)

        df = dlf / f - jnp.concatenate(dk_l, axis=1)
        dfl = df * (1.0 - lb) * sg * (1.0 - sg)
        dlb = jnp.sum(df * (1.0 - sg), axis=0, keepdims=True)
        dl0 = dlb * lb * (1.0 - lb)
        dlg_ref[...] += jnp.concatenate([dl0, -dl0], axis=0)[None]
        if second:
            sq = _sig(qraw)
            dqr = (dqp_ref[...] + dq_t) * (HGD ** -0.5) * (sq * (1.0 + qraw * (1.0 - sq)))
            dp_ref[...] = jnp.concatenate([dfl, dvp_ref[...] + dv_t, dqr], axis=1).astype(BF16)
        else:
            dp_ref[...] = dfl.astype(BF16)
            dv_ref[...] = dv_t
            dq_ref[...] = dq_t

    def col(cb):
        return pl.BlockSpec((TM, HGW), lambda s: (tile_of(s), cb))

    tok = pl.BlockSpec((TM, HGW), lambda s: (tile_of(s), 0))
    in_specs = [col(C_FB if rev else C_FF), col(C_INP), col(C_QHG),
                pl.BlockSpec((1, 2, HGW), lambda s: (d, 0, 0)),
                pl.BlockSpec((TM, HGW), lambda s: (jnp.maximum(tile_of(s) - 1, 0), 0)),
                pl.BlockSpec((ncht, 4, HGD, HGD), lambda s: (tile_of(s), 0, 0, 0))]
    args = [p, p, p, lg, do, st]
    dlg_spec = _full((1, 2, HGW))
    dlg_shape = jax.ShapeDtypeStruct((1, 2, HGW), F32)
    if second:
        in_specs += [tok, tok]
        args += [prev[0], prev[1]]
        out_specs = [pl.BlockSpec((TM, 3 * HGW), lambda s: (tile_of(s), 0)), dlg_spec]
        out_shape = [jax.ShapeDtypeStruct(dp.shape, BF16), dlg_shape]
    else:
        out_specs = [pl.BlockSpec((TM, HGW), lambda s: (tile_of(s), C_FB if rev else C_FF)), tok, tok, dlg_spec]
        out_shape = [jax.ShapeDtypeStruct(dp.shape, BF16), jax.ShapeDtypeStruct((tt, HGW), F32),
                     jax.ShapeDtypeStruct((tt, HGW), F32), dlg_shape]
    in_specs.append(ANY)
    args.append(dp)
    return _pcall(body, name="hgrn_bwd_rev" if rev else "hgrn_bwd", grid=(nt,),
                  in_specs=in_specs, out_specs=out_specs, out_shape=out_shape,
                  scratch=[pltpu.VMEM((4, HGD, HGD), F32)],
                  aliases={len(args) - 1: 0}, carry=carry)(*args)


def _head_rms(o, w, nheads):
    outs = []
    for h in range(nheads):
        oh = o[:, h * HGD:(h + 1) * HGD]
        outs.append(oh * lax.rsqrt(jnp.mean(oh * oh, axis=-1, keepdims=True) + EPS))
    return jnp.concatenate(outs, axis=1)


def _readout_bwd(o0, o1, p, hw4, dy, dp, carry=None):
    tt = o0.shape[0]
    s_len = tt - L

    def body(o0_ref, o1_ref, g_ref, w_ref, dy_ref, _dp_in, dp_ref, do_ref, dw_ref):
        i = pl.program_id(0)

        @pl.when(i == 0)
        def _():
            dw_ref[...] = jnp.zeros_like(dw_ref)
            dp_ref[...] = jnp.zeros_like(dp_ref)

        @pl.when(i >= 1)
        def _():
            o = o0_ref[...] + o1_ref[...]
            g = g_ref[...]
            w = w_ref[...]
            sg = _sig(g)
            dy_ = dy_ref[...]
            dsw = dy_ * (g * sg)
            outs, xhs = [], []
            for h in range(4):
                sl = slice(h * HGD, (h + 1) * HGD)
                oh = o[:, sl]
                r = lax.rsqrt(jnp.mean(oh * oh, axis=-1, keepdims=True) + EPS)
                xh = oh * r
                dxh = dsw[:, sl] * w[:, sl]
                outs.append(r * (dxh - xh * jnp.mean(dxh * xh, axis=-1, keepdims=True)))
                xhs.append(xh)
            xh = jnp.concatenate(xhs, axis=1)
            do_ref[...] = jnp.concatenate(outs, axis=1)
            dp_ref[...] = (dy_ * xh * w * (sg * (1.0 + g * (1.0 - sg)))).astype(BF16)
            dw_ref[...] += jnp.sum(dsw * xh, axis=0, keepdims=True)

    tok = pl.BlockSpec((TM, HGW), lambda i: (i, 0))
    lat = pl.BlockSpec((TM, HGW), lambda i: (jnp.maximum(i - 1, 0), 0))
    return _pcall(body, name="readout_bwd", grid=(tt // TM,),
                  in_specs=[tok, tok, pl.BlockSpec((TM, HGW), lambda i: (i, C_GHG)), _full((1, HGW)), lat, ANY],
                  out_specs=[pl.BlockSpec((TM, HGW), lambda i: (i, C_GHG)), lat, _full((1, HGW))],
                  out_shape=[jax.ShapeDtypeStruct(dp.shape, BF16), jax.ShapeDtypeStruct((s_len, HGW), F32),
                             jax.ShapeDtypeStruct((1, HGW), F32)],
                  aliases={5: 0}, carry=carry)(o0, o1, p, hw4, dy, dp)


def _rope_tables(s_len):
    t = np.arange(s_len)
    inv = ROPE_THETA ** (-np.arange(0, 32, 2, dtype=np.float64) / 32)
    def half(pos):
        ang = pos[:, None].astype(np.float64) * inv[None, :]
        return (np.concatenate([np.cos(ang), np.cos(ang)], 1), np.concatenate([-np.sin(ang), np.sin(ang)], 1))
    cr, sr = half(t // GRID_W)
    cc, sc = half(t % GRID_W)
    cos = np.concatenate([cr, cc, cr, cc], 1)
    sin = np.concatenate([sr, sc, sr, sc], 1)
    cos = np.concatenate([np.ones((L, 128)), cos], 0)
    sin = np.concatenate([np.zeros((L, 128)), sin], 0)
    return jnp.asarray(cos, F32), jnp.asarray(sin, F32)


def _blockdiag(n, w):
    i = np.arange(n)
    return jnp.asarray((i[:, None] // w == i[None, :] // w) / float(w), F32)


def _dup_matrix():
    m = np.zeros((128, 512), np.float32)
    for g in range(2):
        for j in range(4):
            for dd in range(HDIM):
                m[64 * g + dd, 256 * g + 64 * j + dd] = 1.0
    return m


def _head_mean(x, blockdiag):
    return _dot(x, blockdiag, prec=lax.Precision.HIGH)


def _rot(x):
    n = x.shape[1]
    lane = lax.broadcasted_iota(jnp.int32, x.shape, 1)
    return jnp.where((lane % 32) < 16, pltpu.roll(x, n - 16, 1), pltpu.roll(x, 16, 1))


def _qk_prep(p, cos, sin, qnw8, knw2, bd512, bd128, dup, carry=None):
    tt = p.shape[0]

    def body(q_ref, kv_ref, cos_ref, sin_ref, qw_ref, kw_ref, b5_ref, b1_ref, dup_ref,
             qr_ref, k4_ref, v4_ref):
        cos_, sin_ = cos_ref[...], sin_ref[...]
        q = q_ref[...]
        qn = q * lax.rsqrt(_head_mean(q * q, b5_ref[...]) + EPS) * qw_ref[...]
        cos4 = jnp.concatenate([cos_] * 4, axis=1)
        sin4 = jnp.concatenate([sin_] * 4, axis=1)
        qr_ref[...] = ((qn * cos4 + _rot(qn) * sin4) * (HDIM ** -0.5)).astype(BF16)
        kv = kv_ref[...]
        k, v = kv[:, :128], kv[:, 128:]
        kn = k * lax.rsqrt(_head_mean(k * k, b1_ref[...]) + EPS) * kw_ref[...]
        kr = kn * cos_ + _rot(kn) * sin_
        k4_ref[...] = _bdot(kr, dup_ref[...]).astype(BF16)
        v4_ref[...] = _bdot(v, dup_ref[...]).astype(BF16)

    row = lambda w, cb: pl.BlockSpec((TM, w), lambda i: (i, cb))
    out = jax.ShapeDtypeStruct((tt, ATW), BF16)
    return _pcall(body, name="qk_prep", grid=(tt // TM,),
                  in_specs=[row(ATW, C_QRAW), row(256, C_KV), row(128, 0), row(128, 0),
                            _full((1, ATW)), _full((1, 128)), _full((ATW, ATW)), _full((128, 128)),
                            _full((128, ATW))],
                  out_specs=[row(ATW, 0)] * 3, out_shape=[out] * 3, carry=carry)(
                      p, p, cos, sin, qnw8, knw2, bd512, bd128, dup)


def _attn_masks(i, nb):
    r = lax.broadcasted_iota(jnp.int32, (4 * BLK, 3 * BLK + L), 0) % BLK
    c = lax.broadcasted_iota(jnp.int32, (4 * BLK, 3 * BLK + L), 1)
    kpos = (i - 1) * BLK + c
    loc = (jnp.abs(c - BLK - r) <= BLK) & (kpos >= 0) & (kpos < nb * BLK)
    return loc | (c >= 3 * BLK)


def _stack_mask():
    r = lax.broadcasted_iota(jnp.int32, (4 * BLK, 256), 0)
    lane = lax.broadcasted_iota(jnp.int32, (4 * BLK, 256), 1)
    return (r // BLK) == (lane // HDIM)


def _stack_heads(xg, fill=0.0):
    x4 = jnp.concatenate([xg] * 4, axis=0)
    return jnp.where(_stack_mask(), x4, jnp.full_like(x4, fill))


def _unstack_heads(x4):
    out = jnp.where(_lane_mask(0), x4[0:BLK], 0.0)
    for j in range(1, 4):
        out = out + jnp.where(_lane_mask(j), x4[j * BLK:(j + 1) * BLK], 0.0)
    return out


def _per_head_rows(vals):
    return jnp.concatenate([jnp.broadcast_to(v, (BLK, 1)) for v in vals], axis=0)


def _lane_mask(j):
    lane = lax.broadcasted_iota(jnp.int32, (1, 256), 1)
    return (lane // HDIM) == j


def _attn_specs(nb):
    blk = lambda off: pl.BlockSpec((BLK, ATW), lambda i: (jnp.clip(i + off, 0, nb - 1) + 2, 0))
    ctx = pl.BlockSpec((L, ATW), lambda i: (0, 0))
    return blk, ctx


def _attn_fwd(qr, k4, v4, sinks, carry=None):
    tt = qr.shape[0]
    s_len = tt - L
    nb = s_len // BLK

    def body(sk_ref, q_ref, kp, ko, kn, kc, vp, vo, vn, vc, y_ref, lse_ref):
        i = pl.program_id(0)
        valid = _attn_masks(i, nb)
        q = q_ref[...]
        ys, lses = [], []
        for g in range(2):
            gs = slice(256 * g, 256 * g + 256)
            kcat = jnp.concatenate([kp[:, gs], ko[:, gs], kn[:, gs], kc[:, gs]], axis=0)
            vcat = jnp.concatenate([vp[:, gs], vo[:, gs], vn[:, gs], vc[:, gs]], axis=0)
            sink4 = _per_head_rows([sk_ref[4 * g + j] for j in range(4)])
            q4 = _stack_heads(q[:, gs])
            o_parts, l_parts = [], []
            for hp in range(2):
                rows = slice(2 * BLK * hp, 2 * BLK * (hp + 1))
                sink = sink4[rows]
                s = jnp.where(valid[rows], _dot(q4[rows], kcat, NT), -1e30)
                m = jnp.maximum(jnp.max(s, axis=-1, keepdims=True), sink)
                e = jnp.exp(s - m)
                den = jnp.sum(e, axis=-1, keepdims=True) + jnp.exp(sink - m)
                o_parts.append(_bdot(e * (1.0 / den), vcat))
                l_parts.append(jnp.broadcast_to(m + jnp.log(den), (2 * BLK, 256)))
            ys.append(_unstack_heads(jnp.concatenate(o_parts, axis=0)))
            lses.append(_unstack_heads(jnp.concatenate(l_parts, axis=0)))
        y_ref[...] = jnp.concatenate(ys, axis=1).astype(BF16)
        lse_ref[...] = jnp.concatenate(lses, axis=1)

    blk, ctx = _attn_specs(nb)
    out = pl.BlockSpec((BLK, ATW), lambda i: (i, 0))
    return _pcall(body, name="attn_fwd", grid=(nb,),
                  in_specs=[pl.BlockSpec(memory_space=pltpu.SMEM), blk(0),
                            blk(-1), blk(0), blk(1), ctx, blk(-1), blk(0), blk(1), ctx],
                  out_specs=[out, out],
                  out_shape=[jax.ShapeDtypeStruct((s_len, ATW), BF16),
                             jax.ShapeDtypeStruct((s_len, ATW), F32)], carry=carry)(
                      sinks, qr, k4, k4, k4, k4, v4, v4, v4, v4)


def _attn_bwd(qr, k4, v4, sinks, y, lse, dy, carry=None):
    tt = qr.shape[0]
    s_len = tt - L
    nb = s_len // BLK

    def body(sk_ref, q_ref, kp, ko, kn, kc, vp, vo, vn, vc, y_ref, lse_ref, dy_ref,
             dq_ref, dkw_ref, dvw_ref, dkc_ref, dvc_ref, dsk_ref):
        i = pl.program_id(0)

        @pl.when(i == 0)
        def _():
            dkc_ref[...] = jnp.zeros_like(dkc_ref)
            dvc_ref[...] = jnp.zeros_like(dvc_ref)
            dsk_ref[...] = jnp.zeros_like(dsk_ref)

        valid = _attn_masks(i, nb)
        q = q_ref[...]
        dy_ = dy_ref[...]
        dly = dy_ * y_ref[...].astype(F32)
        lse_ = lse_ref[...]
        dqs = []
        for g in range(2):
            gs = slice(256 * g, 256 * g + 256)
            kcat = jnp.concatenate([kp[:, gs], ko[:, gs], kn[:, gs], kc[:, gs]], axis=0)
            vcat = jnp.concatenate([vp[:, gs], vo[:, gs], vn[:, gs], vc[:, gs]], axis=0)
            q4 = _stack_heads(q[:, gs])
            dy4 = _stack_heads(dy_[:, gs]).astype(BF16)
            lse4 = jnp.max(_stack_heads(lse_[:, gs], fill=-1e30), axis=-1, keepdims=True)
            delta = jnp.sum(_stack_heads(dly[:, gs]), axis=-1, keepdims=True)
            sink = _per_head_rows([sk_ref[4 * g + j] for j in range(4)])
            pr = jnp.where(valid, jnp.exp(_dot(q4, kcat, NT) - lse4), 0.0)
            dsb = (pr * (_dot(dy4, vcat, NT) - delta)).astype(BF16)
            dsink = jnp.exp(sink - lse4) * delta
            for j in range(4):
                dsk_ref[4 * g + j:4 * g + j + 1, :] += jnp.broadcast_to(
                    -jnp.sum(dsink[j * BLK:(j + 1) * BLK], axis=0, keepdims=True), (1, 128))
            dqs.append(_unstack_heads(_dot(dsb, kcat)))
            dkg = _dot(dsb, q4, TN)
            dvg = _dot(pr.astype(BF16), dy4, TN)
            dkw_ref[0, :, gs] = dkg[:3 * BLK]
            dvw_ref[0, :, gs] = dvg[:3 * BLK]
            dkc_ref[:, gs] += dkg[3 * BLK:]
            dvc_ref[:, gs] += dvg[3 * BLK:]
        dq_ref[...] = jnp.concatenate(dqs, axis=1)

    blk, ctx = _attn_specs(nb)
    out = pl.BlockSpec((BLK, ATW), lambda i: (i, 0))
    win = pl.BlockSpec((1, 3 * BLK, ATW), lambda i: (i, 0, 0))
    acc = _full((L, ATW))
    return _pcall(body, name="attn_bwd", grid=(nb,),
                  in_specs=[pl.BlockSpec(memory_space=pltpu.SMEM), blk(0),
                            blk(-1), blk(0), blk(1), ctx, blk(-1), blk(0), blk(1), ctx, out, out, out],
                  out_specs=[out, win, win, acc, acc, _full((8, 128))],
                  out_shape=[jax.ShapeDtypeStruct((s_len, ATW), F32),
                             jax.ShapeDtypeStruct((nb, 3 * BLK, ATW), F32),
                             jax.ShapeDtypeStruct((nb, 3 * BLK, ATW), F32),
                             jax.ShapeDtypeStruct((L, ATW), F32), jax.ShapeDtypeStruct((L, ATW), F32),
                             jax.ShapeDtypeStruct((8, 128), F32)], carry=carry)(
                      sinks, qr, k4, k4, k4, k4, v4, v4, v4, v4, y, lse, dy)


def _attn_post(p, cos, sin, qnw8, knw2, bd512, bd128, dupt, dq, dkw, dvw, dkc, dvc, dp, carry=None):
    tt = p.shape[0]
    s_len = tt - L
    nb = s_len // BLK
    nctx = L // BLK

    def body(q_ref, kv_ref, cos_ref, sin_ref, qw_ref, kw_ref, b5_ref, b1_ref, dupt_ref,
             dq_ref, kwp, kwo, kwn, vwp, vwo, vwn, dkc_ref, dvc_ref, _dp_in,
             dp_ref, dqw_ref, dkw_ref):
        t = pl.program_id(0)
        j = t - nctx

        @pl.when(t == 0)
        def _():
            dqw_ref[...] = jnp.zeros_like(dqw_ref)
            dkw_ref[...] = jnp.zeros_like(dkw_ref)

        is_lat = t >= nctx
        cos_, sin_ = cos_ref[...], sin_ref[...]
        has_p = is_lat & (j >= 1)
        has_n = is_lat & (j <= nb - 2)
        dk4 = (jnp.where(is_lat, kwo[0], dkc_ref[...]) + jnp.where(has_p, kwp[0], 0.0)
               + jnp.where(has_n, kwn[0], 0.0))
        dv4 = (jnp.where(is_lat, vwo[0], dvc_ref[...]) + jnp.where(has_p, vwp[0], 0.0)
               + jnp.where(has_n, vwn[0], 0.0))
        dkr = _dot(dk4, dupt_ref[...], prec=HI)
        dv = _dot(dv4, dupt_ref[...], prec=HI)
        kv = kv_ref[...]
        k = kv[:, :128]
        kw = kw_ref[...]
        rk = lax.rsqrt(_head_mean(k * k, b1_ref[...]) + EPS)
        xk = k * rk
        dkn = dkr * cos_ + _rot(dkr * sin_)
        dxk = dkn * kw
        dk = rk * (dxk - xk * _head_mean(dxk * xk, b1_ref[...]))
        dkw_ref[...] += jnp.sum(dkn * xk, axis=0, keepdims=True)
        q = q_ref[...]
        qw = qw_ref[...]
        rq = lax.rsqrt(_head_mean(q * q, b5_ref[...]) + EPS)
        xq = q * rq
        cos4 = jnp.concatenate([cos_] * 4, axis=1)
        sin4 = jnp.concatenate([sin_] * 4, axis=1)
        dqr = jnp.where(is_lat, dq_ref[...], 0.0) * (HDIM ** -0.5)
        dqn = dqr * cos4 + _rot(dqr * sin4)
        dxq = dqn * qw
        dqraw = rq * (dxq - xq * _head_mean(dxq * xq, b5_ref[...]))
        dqw_ref[...] += jnp.sum(dqn * xq, axis=0, keepdims=True)
        dp_ref[...] = jnp.concatenate([dqraw, dk, dv], axis=1).astype(BF16)

    row = lambda w, cb: pl.BlockSpec((BLK, w), lambda t: (t, cb))
    lat = pl.BlockSpec((BLK, ATW), lambda t: (jnp.maximum(t - nctx, 0), 0))

    def part(off):
        return pl.BlockSpec((1, BLK, ATW), lambda t: (jnp.clip(t - nctx + off, 0, nb - 1), 1 - off, 0))

    cacc = pl.BlockSpec((BLK, ATW), lambda t: (jnp.minimum(t, nctx - 1), 0))
    return _pcall(body, name="attn_post", grid=(tt // BLK,),
                  in_specs=[row(ATW, C_QRAW), row(256, C_KV), row(128, 0), row(128, 0),
                            _full((1, ATW)), _full((1, 128)), _full((ATW, ATW)), _full((128, 128)),
                            _full((ATW, 128)), lat, part(-1), part(0), part(1), part(-1), part(0), part(1),
                            cacc, cacc, ANY],
                  out_specs=[pl.BlockSpec((BLK, 768), lambda t: (t, C_QKV)), _full((1, ATW)), _full((1, 128))],
                  out_shape=[jax.ShapeDtypeStruct(dp.shape, BF16), jax.ShapeDtypeStruct((1, ATW), F32),
                             jax.ShapeDtypeStruct((1, 128), F32)],
                  aliases={18: 0}, carry=carry)(p, p, cos, sin, qnw8, knw2, bd512, bd128, dupt,
                                   dq, dkw, dkw, dkw, dvw, dvw, dvw, dkc, dvc, dp)


def _branch_merge(y_hg, y_at, bh4, ba4, p):
    s_len = y_hg.shape[0]

    def body(yh_ref, ya_ref, bh_ref, ba_ref, gh_ref, ga_ref, ah_ref, aa_ref, m_ref):
        yh, ya = yh_ref[...], ya_ref[...]
        ah = jnp.concatenate([_bdot(yh, bh_ref[j]) for j in range(4)], axis=1)
        aa = jnp.concatenate([_bdot(ya, ba_ref[j]) for j in range(4)], axis=1)
        ah_ref[...] = ah.astype(BF16)
        aa_ref[...] = aa.astype(BF16)
        m_ref[...] = (_sig(gh_ref[...]) * ah + _sig(ga_ref[...]) * aa).astype(BF16)

    row = pl.BlockSpec((TM, D), lambda i: (i, 0))
    y = pl.BlockSpec((TM, HGW), lambda i: (i, 0))
    f = jax.ShapeDtypeStruct((s_len, D), BF16)
    return _pcall(body, name="branch_merge", grid=(s_len // TM,),
                  in_specs=[y, y, _full(bh4.shape), _full(ba4.shape),
                            pl.BlockSpec((TM, D), lambda i: (i + 1, 2)), pl.BlockSpec((TM, D), lambda i: (i + 1, 3))],
                  out_specs=[row, row, row],
                  out_shape=[f, f, jax.ShapeDtypeStruct((s_len, D), BF16)])(y_hg, y_at, bh4, ba4, p, p)


def _branch_bwd(dmh, dma, bh4, ba4, y_hg, y_at):
    s_len = dmh.shape[0]
    nk = s_len // TS
    ns = D // 4

    def body(dh_ref, da_ref, bh_ref, ba_ref, yh_ref, ya_ref, dyh_ref, dya_ref, gh_ref, ga_ref, acc_h, acc_a):
        t = pl.program_id(0)

        @pl.when(t == 0)
        def _():
            acc_h[...] = jnp.zeros_like(acc_h)
            acc_a[...] = jnp.zeros_like(acc_a)

        for d_ref, w_ref, y_ref, dy_ref, acc in ((dh_ref, bh_ref, yh_ref, dyh_ref, acc_h),
                                                 (da_ref, ba_ref, ya_ref, dya_ref, acc_a)):
            y = y_ref[...]
            dy = jnp.zeros((TS, HGW), F32)
            for j in range(4):
                dj = d_ref[:, j * ns:(j + 1) * ns]
                dy = dy + _bdot(dj, w_ref[j], NT)
                acc[j] += _bdot(y, dj, TN)
            dy_ref[...] = dy

        @pl.when(t == nk - 1)
        def _():
            gh_ref[...] = acc_h[...].astype(BF16)
            ga_ref[...] = acc_a[...].astype(BF16)

    dm = pl.BlockSpec((TS, D), lambda t: (t, 0))
    y = pl.BlockSpec((TS, HGW), lambda t: (t, 0))
    w = _full(bh4.shape)
    fy = jax.ShapeDtypeStruct((s_len, HGW), F32)
    gw = jax.ShapeDtypeStruct(bh4.shape, BF16)
    return _pcall(body, name="branch_bwd", grid=(nk,), in_specs=[dm, dm, w, w, y, y],
                  out_specs=[y, y, w, w], out_shape=[fy, fy, gw, gw],
                  scratch=[pltpu.VMEM(bh4.shape, F32)] * 2)(dmh, dma, bh4, ba4, y_hg, y_at)


def _merge_bwd(dattn, w_o, mixed, ah, aa, p, carry=None):
    tt = p.shape[0]
    s_len = tt - L
    nt = tt // TM

    def body(da_ref, wo_ref, mx_ref, ah_ref, aa_ref, gh_ref, ga_ref, dp_ref, dmh_ref, dma_ref, go_ref, acc):
        i = pl.program_id(0)

        @pl.when(i == 0)
        def _():
            dp_ref[...] = jnp.zeros_like(dp_ref)
            acc[...] = jnp.zeros_like(acc)

        @pl.when(i >= 1)
        def _():
            da = da_ref[...]
            acc[...] += _bdot(mx_ref[...], da, TN)
            dm_ = _bdot(da, wo_ref[...], NT)
            sh, sa = _sig(gh_ref[...]), _sig(ga_ref[...])
            dp_ref[...] = jnp.concatenate([dm_ * ah_ref[...].astype(F32) * sh * (1.0 - sh),
                                           dm_ * aa_ref[...].astype(F32) * sa * (1.0 - sa)], axis=1).astype(BF16)
            dmh_ref[...] = (dm_ * sh).astype(BF16)
            dma_ref[...] = (dm_ * sa).astype(BF16)

        @pl.when(i == nt - 1)
        def _():
            go_ref[...] = acc[...].astype(BF16)

    lat = pl.BlockSpec((TM, D), lambda i: (jnp.maximum(i - 1, 0), 0))
    return _pcall(body, name="merge_bwd", grid=(nt,),
                  in_specs=[lat, _full((D, D)), lat, lat, lat, pl.BlockSpec((TM, D), lambda i: (i, 2)),
                            pl.BlockSpec((TM, D), lambda i: (i, 3))],
                  out_specs=[pl.BlockSpec((TM, 2 * D), lambda i: (i, C_GATES)), lat, lat, _full((D, D))],
                  out_shape=[jax.ShapeDtypeStruct((tt, NCOL), BF16), jax.ShapeDtypeStruct((s_len, D), BF16),
                             jax.ShapeDtypeStruct((s_len, D), BF16), jax.ShapeDtypeStruct((D, D), BF16)],
                  scratch=[pltpu.VMEM((D, D), F32)], carry=carry)(dattn, w_o, mixed, ah, aa, p, p)


def _local_step(x, ctx, tgt, mod, modc, nw1, nw2, lg, hw, qnw, knw, sinks,
                w_in, wts, dist=None):
    s_len = x.shape[0]
    tt = s_len + L
    ss1 = jnp.stack([modc, mod[0:2]])
    ss2 = mod[3:5][None]
    g1, g2 = mod[2:3], mod[5:6]
    hw4 = jnp.tile(hw, (1, 4))
    qnw8 = jnp.tile(qnw, (1, 8))
    knw2 = jnp.tile(knw, (1, 2))
    cos, sin = _rope_tables(s_len)
    bd512, bd128 = _blockdiag(ATW, HDIM), _blockdiag(128, HDIM)
    dupm = _dup_matrix()
    dup, dupt = jnp.asarray(dupm, BF16), jnp.asarray(dupm.T, F32)
    tmt = tt

    def four(b):
        return b.reshape(4, 2 * b.shape[1], b.shape[2])

    def halves(g):
        return g.reshape(4, 2, g.shape[1] // 2, g.shape[2])

    h = _mod1(ctx, x, nw1, ss1)
    if dist is None:
        bh4, ba4, w_o, g4, u4, dn4 = wts
        p = _mm_in(h, w_in, tmt)
        o0, st0 = _hgrn_fwd(p, lg, rev=False)
        o1, st1, y_hg = _hgrn_fwd(p, lg, rev=True, readout=(o0, hw4))
    else:
        core, chip = dist
        half = wts[3].shape[1] // 2
        p, (o8, g8a) = _mm_in(h, w_in, tmt, carry=_carry_gather([wts[2], wts[3]], rows=[None, (0, half)]))
        (o0, st0), (g8,) = _hgrn_fwd(p, lg, rev=False, carry=_carry_gather([g8a], rows=[(half, half)]))
        (o1, st1, y_hg), (dn8a, bh8, ba8) = _hgrn_fwd(
            p, lg, rev=True, readout=(o0, hw4),
            carry=_carry_gather([wts[5], wts[0], wts[1]], rows=[(0, half), None, None]))
        bh4, ba4, w_o, g4 = four(bh8), four(ba8), four(o8).reshape(D, D), four(g8)
    if dist is None:
        qr, k4, v4 = _qk_prep(p, cos, sin, qnw8, knw2, bd512, bd128, dup)
        y_at, lse = _attn_fwd(qr, k4, v4, sinks)
    else:
        (qr, k4, v4), (dn8,) = _qk_prep(p, cos, sin, qnw8, knw2, bd512, bd128, dup,
                                        carry=_carry_gather([dn8a], rows=[(half, half)]))
        (y_at, lse), (u8,) = _attn_fwd(qr, k4, v4, sinks, carry=_carry_gather([wts[4]]))
        u4, dn4 = four(u8), four(dn8)
    ah, aa, mixed = _branch_merge(y_hg, y_at, bh4, ba4, p)
    ao, x1, h2 = _out_proj_mod2(mixed, w_o, x, g1, nw2, ss2)
    a4, b4, z4 = _ffn_up(h2, g4, u4)
    sq, dx2, dyb, dg2 = _ffn_down_loss(z4, dn4, x1, g2, tgt)

    da4, db4 = _ffn_dz(dyb, dn4, a4, b4)
    g_dn = _ffn_gdn(z4, dyb)
    if dist is None:
        dh2 = _ffn_dh2(da4, db4, g4, u4)
    else:
        dn_units = [halves(g_dn)]
        dh2, dn_recv = _ffn_dh2(da4, db4, g4, u4, carry=_carry_pairx(dn_units))
        dn_pairs = _rs_pair_add(dn_units, dn_recv, core)
    if dist is None:
        g_g, g_u = _ffn_ggu(h2, da4, db4)
    else:
        (g_g, g_u), c_dn = _ffn_ggu(h2, da4, db4, carry=_carry_chipx(dn_pairs))
        red_dn = _rs_chip_add(dn_pairs, c_dn, core, chip)
    dx1, dattn, dss2, dnw2, dg1 = _mod2_bwd(x1, dh2, dx2, ao, nw2, ss2, g1)
    if dist is None:
        dp, dmh, dma, g_o = _merge_bwd(dattn, w_o, mixed, ah, aa, p)
    else:
        gu_units = [halves(g_g), halves(g_u)]
        (dp, dmh, dma, g_o), gu_recv = _merge_bwd(dattn, w_o, mixed, ah, aa, p, carry=_carry_pairx(gu_units))
        ffn_pairs = list(dn_pairs) + list(_rs_pair_add(gu_units, gu_recv, core))
    dy_hg, dy_at, g_bh, g_ba = _branch_bwd(dmh, dma, bh4, ba4, y_hg, y_at)
    if dist is None:
        dp, do, dhw4 = _readout_bwd(o0, o1, p, hw4, dy_hg, dp)
        dq, dkw, dvw, dkc, dvc, dsk = _attn_bwd(qr, k4, v4, sinks, y_at, lse, dy_at)
        dp, dqnw8, dknw2 = _attn_post(p, cos, sin, qnw8, knw2, bd512, bd128, dupt, dq, dkw, dvw, dkc, dvc, dp)
    else:
        mix_units = [halves(g_bh), halves(g_ba), halves(g_o.reshape(4, D // 4, D))]
        (dp, do, dhw4), mix_recv = _readout_bwd(o0, o1, p, hw4, dy_hg, dp, carry=_carry_pairx(mix_units))
        mix_pairs = _rs_pair_add(mix_units, mix_recv, core)
        (dq, dkw, dvw, dkc, dvc, dsk), bwd = _attn_bwd(
            qr, k4, v4, sinks, y_at, lse, dy_at,
            carry=_carry_join(_carry_chipx(ffn_pairs[1:2]), _carry_sibx(red_dn)))
        red_g = _rs_chip_add(ffn_pairs[1:2], bwd[0:1], core, chip)
        (dp, dqnw8, dknw2), post = _attn_post(
            p, cos, sin, qnw8, knw2, bd512, bd128, dupt, dq, dkw, dvw, dkc, dvc, dp, carry=_carry_sibx(red_g))
    if dist is None:
        dp, dv0, dq0, dlg0 = _hgrn_bwd(p, lg, do, st0, dp, None, rev=False)
        dp, dlg1 = _hgrn_bwd(p, lg, do, st1, dp, (dv0, dq0), rev=True)
    else:
        (dp, dv0, dq0, dlg0), c_u = _hgrn_bwd(p, lg, do, st0, dp, None, rev=False,
                                              carry=_carry_chipx(ffn_pairs[2:3]))
        red_u = _rs_chip_add(ffn_pairs[2:3], c_u, core, chip)
        (dp, dlg1), last = _hgrn_bwd(p, lg, do, st1, dp, (dv0, dq0), rev=True,
                                     carry=_carry_join(_carry_chipx(mix_pairs), _carry_sibx(red_u)))
        mix_reds = _rs_chip_add(mix_pairs, last[0:3], core, chip)
        ffn_done = bwd[1:2] + post[0:1] + last[3:4]
    g_in = _mm_gin(dp, h, tmt)
    if dist is None:
        dh = _mm_dh(dp, w_in, tmt)
        gx, dss1, dnw1 = _mod1_bwd(ctx, x, dh, dx1, nw1, ss1)
        rs = None
    else:
        in_units = [halves(g_in.reshape(4, NCOL // 4, D))]
        in_pairs = _rs_pair_add(in_units, _pair_exchange(in_units), core)
        first_rows, rest_rows = _split_rows(in_pairs[0].shape[1], IN_ROWS_WITH_DH)
        dh, both = _mm_dh(dp, w_in, tmt, carry=_carry_join(_carry_chipx(in_pairs, rows=first_rows),
                                                           _carry_sibx(mix_reds)))
        in_part, mix_done = both[0:1], both[1:4]
        gx, dss1, dnw1 = _mod1_bwd(ctx, x, dh, dx1, nw1, ss1)
        rs = dict(ffn_done=ffn_done, mix_done=mix_done, in_pairs=in_pairs, in_part=in_part, rest_rows=rest_rows)

    dmod = jnp.concatenate([dss1[1], dg1, dss2, dg2], axis=0)
    dmodc = dss1[0]
    raw = (dss1, dg1, dss2, dg2, dnw1, dnw2, dhw4, dqnw8, dknw2, dsk, dlg0, dlg1)
    small = dict(raw=raw, dmod=dmod, dmodc=dmodc, dnw1=dnw1, dnw2=dnw2,
                 dhw=dhw4.reshape(4, HGD).sum(0, keepdims=True),
                 dqnw=dqnw8.reshape(8, HDIM).sum(0, keepdims=True),
                 dknw=dknw2.reshape(2, HDIM).sum(0, keepdims=True),
                 dsinks=dsk[:, 0], dlg=jnp.concatenate([dlg0, dlg1], axis=0))
    big = dict(w_in=g_in, w_bh=g_bh, w_ba=g_ba, w_o=g_o, w_g=g_g, w_u=g_u, w_dn=g_dn)
    return sq, gx, big, small, rs


def _place():
    x, y, c = lax.axis_index("x"), lax.axis_index("y"), lax.axis_index("c")
    return x, y, c


def _gather_blocks(x_refs, out_refs, send_sems, recv_sems, local_sems):
    n = len(out_refs)
    x, y, c = _place()
    me, sibling = (x, y, c), (x, y, 1 - c)
    chips = [(1 - x, y), (x, 1 - y), (1 - x, 1 - y)]

    def slot(u, px, py, pc):
        return out_refs[u].at[4 * px + 2 * py + pc]

    def copy(u, k, block, to, src=None):
        return pltpu.make_async_remote_copy(
            src_ref=slot(u, *block) if src is None else src, dst_ref=slot(u, *block),
            send_sem=send_sems.at[u, k], recv_sem=recv_sems.at[u, k], device_id=to, device_id_type=MESH)

    mines = [pltpu.make_async_copy(x_refs[u], slot(u, *me), local_sems.at[u]) for u in range(n)]
    for cp in mines:
        cp.start()
    first = []
    for u in range(n):
        first.append(copy(u, 0, me, sibling, src=x_refs[u]))
        first += [copy(u, 1 + j, me, (*chip, c), src=x_refs[u]) for j, chip in enumerate(chips)]
    for cp in first:
        cp.start()
    passed = []
    for j, chip in enumerate(chips):
        for u in range(n):
            copy(u, 1 + j, (*chip, c), me).wait_recv()
            fwd = copy(u, 4 + j, (*chip, c), sibling)
            fwd.start()
            passed.append(fwd)
    for u in range(n):
        copy(u, 0, sibling, me).wait_recv()
    for j, chip in enumerate(chips):
        for u in range(n):
            copy(u, 4 + j, (*chip, 1 - c), me).wait_recv()
    for cp in first + passed:
        cp.wait_send()
    for cp in mines:
        cp.wait()


def _gather_sems(n):
    return [pltpu.SemaphoreType.DMA((n, 7)), pltpu.SemaphoreType.DMA((n, 7)), pltpu.SemaphoreType.DMA((n,))]


def _cast_place(ws, c, dev):
    n = len(ws)

    def body(s_ref, *refs):
        for u in range(n):
            refs[n + u][0] = refs[u][...].astype(BF16)

    in_specs, out_specs, out_shape = [], [], []
    for w in ws:
        q, cols = w.shape[0] // 4, w.shape[1]
        in_specs.append(pl.BlockSpec((q, cols), lambda i, s: (2 * s[0] + i, 0)))
        out_specs.append(pl.BlockSpec((1, q, cols), lambda i, s: (s[1], i, 0)))
        out_shape.append(jax.ShapeDtypeStruct((8, 2 * q, cols), BF16))
    return pl.pallas_call(
        body, name="cast_place",
        grid_spec=pltpu.PrefetchScalarGridSpec(num_scalar_prefetch=1, grid=(2,), in_specs=in_specs,
                                               out_specs=out_specs),
        out_shape=_out_hbm(out_shape),
        compiler_params=pltpu.CompilerParams(vmem_limit_bytes=48 << 20))(jnp.stack([c, dev]), *_in_hbm(ws))


def _gather_phases(out_refs, send_sems, recv_sems, rows=None):
    n = len(out_refs)
    x, y, c = _place()
    me, sibling = (x, y, c), (x, y, 1 - c)
    chips = [(1 - x, y), (x, 1 - y), (1 - x, 1 - y)]

    def copy(u, k, block, to):
        px, py, pc = block
        ref = out_refs[u].at[4 * px + 2 * py + pc]
        if rows is not None and rows[u] is not None:
            ref = ref.at[pl.ds(rows[u][0], rows[u][1])]
        return pltpu.make_async_remote_copy(src_ref=ref, dst_ref=ref, send_sem=send_sems.at[u, k],
                                            recv_sem=recv_sems.at[u, k], device_id=to, device_id_type=MESH)

    def start():
        for u in range(n):
            copy(u, 0, me, sibling).start()
            for j, chip in enumerate(chips):
                copy(u, 1 + j, me, (*chip, c)).start()

    def mid():
        for j, chip in enumerate(chips):
            for u in range(n):
                copy(u, 1 + j, (*chip, c), me).wait_recv()
                copy(u, 4 + j, (*chip, c), sibling).start()

    def end():
        for u in range(n):
            copy(u, 0, sibling, me).wait_recv()
        for j, chip in enumerate(chips):
            for u in range(n):
                copy(u, 4 + j, (*chip, 1 - c), me).wait_recv()
        for u in range(n):
            copy(u, 0, me, sibling).wait_send()
            for j, chip in enumerate(chips):
                copy(u, 1 + j, me, (*chip, c)).wait_send()
                copy(u, 4 + j, (*chip, c), sibling).wait_send()

    return start, mid, end


def _carry_gather(bufs, rows=None):
    n = len(bufs)
    return _Carry(bufs, [jax.ShapeDtypeStruct(b.shape, b.dtype) for b in bufs], {u: u for u in range(n)},
                  [pltpu.SemaphoreType.DMA((n, 7)), pltpu.SemaphoreType.DMA((n, 7))],
                  lambda ins, outs, sems: _gather_phases(outs, *sems, rows=rows), "both")


def _ag_small(raw, sq):
    def body(dss1, dg1, dss2, dg2, dnw1, dnw2, dhw4, dqnw8, dknw2, dsk, dlg0, dlg1, sq_ref,
             out_ref, tot_ref, blk, send_sems, recv_sems, local_sems):
        _peer_barrier("both")
        blk[...] = jnp.zeros_like(blk)
        blk[0:2, :] = dss1[1]
        blk[2:3, :] = dg1[...]
        blk[3:5, :] = dss2[...]
        blk[5:6, :] = dg2[...]
        blk[6:8, :] = dss1[0]
        blk[8:9, :] = dnw1[...]
        blk[9:10, :] = dnw2[...]
        blk[10:11, 0:HGW] = dhw4[...]
        blk[10:11, HGW:D] = dqnw8[...]
        blk[11:12, 0:128] = dknw2[...]
        blk[12:14, 0:HGW] = dlg0[0]
        blk[14:16, 0:HGW] = dlg1[0]
        blk[16:24, 0:128] = dsk[...]
        blk[24:25, :] = sq_ref[...]
        _gather_blocks([blk], [out_ref], send_sems, recv_sems, local_sems)
        acc = out_ref[0]
        for i in range(1, 8):
            acc = acc + out_ref[i]
        tot_ref[...] = acc

    vm = pl.BlockSpec(memory_space=pltpu.VMEM)
    return pl.pallas_call(
        body, name="ag_small",
        out_shape=[jax.ShapeDtypeStruct((8, 32, D), F32), jax.ShapeDtypeStruct((32, D), F32)],
        in_specs=[vm] * 13, out_specs=[vm, vm],
        scratch_shapes=[pltpu.VMEM((32, D), F32)] + _gather_sems(1),
        compiler_params=pltpu.CompilerParams(collective_id=BARRIER_IDS["both"]))(*raw, sq)


def _pairx_shapes(units):
    return [jax.ShapeDtypeStruct((4,) + g.shape[2:], g.dtype) for g in units]


def _pairx_phases(g_refs, r_refs, send_sems, recv_sems):
    n = len(g_refs)
    x, y, c = _place()
    cps = [pltpu.make_async_remote_copy(
        src_ref=g_refs[u].at[j, 1 - c], dst_ref=r_refs[u].at[j], send_sem=send_sems.at[u, j],
        recv_sem=recv_sems.at[u, j], device_id=(x, y, 1 - c), device_id_type=MESH)
        for u in range(n) for j in range(4)]

    def start():
        for cp in cps:
            cp.start()

    def end():
        for cp in cps:
            cp.wait()

    return start, None, end


def _carry_pairx(units):
    n = len(units)
    return _Carry(units, _pairx_shapes(units), {},
                  [pltpu.SemaphoreType.DMA((n, 4)), pltpu.SemaphoreType.DMA((n, 4))],
                  lambda ins, outs, sems: _pairx_phases(ins, outs, *sems), "sib")


def _pair_exchange(units):
    n = len(units)

    def body(*refs):
        _peer_barrier("sib")
        start, _, end = _pairx_phases(refs[:n], refs[n:2 * n], *refs[2 * n:])
        start()
        end()

    return pl.pallas_call(
        body, name="pair_exchange", out_shape=_pairx_shapes(units), in_specs=[ANY] * n, out_specs=[ANY] * n,
        scratch_shapes=[pltpu.SemaphoreType.DMA((n, 4))] * 2,
        compiler_params=pltpu.CompilerParams(collective_id=BARRIER_IDS["sib"]))(*_in_hbm(units))


IN_ROWS_WITH_DH = 0.6


def _split_rows(h, share):
    first = int(h * share) // BF16_SUBLANES * BF16_SUBLANES
    return (0, first), (first, h - first)


def _rs_pair_add(units, recvs, c):
    n = len(units)

    def body(c_ref, *refs):
        for u in range(n):
            refs[2 * n + u][...] = (refs[u][0].astype(F32) + refs[n + u][...].astype(F32)).astype(BF16)

    in_specs, out_specs, out_shape = [], [], []
    for g in units:
        h, w = g.shape[2:]
        in_specs.append(pl.BlockSpec((1, 1, h, w), lambda j, cr: (j, cr[0], 0, 0)))
    for g in units:
        h, w = g.shape[2:]
        in_specs.append(pl.BlockSpec((1, h, w), lambda j, cr: (j, 0, 0)))
        out_specs.append(pl.BlockSpec((1, h, w), lambda j, cr: (j, 0, 0)))
        out_shape.append(jax.ShapeDtypeStruct((4, h, w), BF16))
    return pl.pallas_call(
        body, name="rs_pair_add",
        grid_spec=pltpu.PrefetchScalarGridSpec(num_scalar_prefetch=1, grid=(4,), in_specs=in_specs,
                                               out_specs=out_specs),
        out_shape=_out_hbm(out_shape),
        compiler_params=pltpu.CompilerParams(vmem_limit_bytes=48 << 20))(
            c.reshape(1), *_in_hbm(list(units) + list(recvs)))


def _chipx_phases(p_refs, r_refs, send_sems, recv_sems, rows=None):
    n = len(p_refs)
    x, y, c = _place()
    k = 2 * x + y

    def part(ref):
        return ref if rows is None else ref.at[pl.ds(rows[0], rows[1])]

    sends = []
    for d in range(1, 4):
        j = (k + d) % 4
        for u in range(n):
            sends.append(pltpu.make_async_remote_copy(
                src_ref=part(p_refs[u].at[j]), dst_ref=part(r_refs[u].at[k]), send_sem=send_sems.at[u, d - 1],
                recv_sem=recv_sems.at[u, d - 1], device_id=(j // 2, j % 2, c), device_id_type=MESH))

    def start():
        for cp in sends:
            cp.start()

    def end():
        for d in range(1, 4):
            src = (k + 4 - d) % 4
            for u in range(n):
                pltpu.make_async_remote_copy(
                    src_ref=part(p_refs[u].at[src]), dst_ref=part(r_refs[u].at[src]),
                    send_sem=send_sems.at[u, d - 1], recv_sem=recv_sems.at[u, d - 1], device_id=(x, y, c),
                    device_id_type=MESH).wait_recv()
        for cp in sends:
            cp.wait_send()

    return start, None, end


def _carry_chipx(pairs, rows=None, into=None):
    n = len(pairs)
    sems = [pltpu.SemaphoreType.DMA((n, 3)), pltpu.SemaphoreType.DMA((n, 3))]
    shapes = [jax.ShapeDtypeStruct(p.shape, p.dtype) for p in pairs]
    if into is None:
        return _Carry(pairs, shapes, {}, sems, lambda ins, outs, s: _chipx_phases(ins, outs, *s, rows=rows),
                      "chips")
    return _Carry(list(pairs) + list(into), shapes, {n + u: u for u in range(n)}, sems,
                  lambda ins, outs, s: _chipx_phases(ins[:n], outs, *s, rows=rows), "chips")


def _rs_chip_add(pairs, contribs, c, chip):
    n = len(pairs)

    def body(s_ref, *refs):
        for u in range(n):
            a, b, c_, d = refs[4 * u:4 * u + 4]
            refs[4 * n + u][0] = ((a[0].astype(F32) + b[0].astype(F32)) + c_[0].astype(F32)) + d[0].astype(F32)

    in_specs, out_specs, out_shape, args = [], [], [], []
    for p, r in zip(pairs, contribs):
        h, w = p.shape[1] // 2, p.shape[2]
        in_specs += [pl.BlockSpec((1, h, w), functools.partial(lambda d, i, s: ((s[1] + d) % 4, i, 0), d))
                     for d in range(4)]
        args += [p, r, r, r]
        out_specs.append(pl.BlockSpec((1, h, w), lambda i, s: (s[0], i, 0)))
        out_shape.append(jax.ShapeDtypeStruct((2, 2 * h, w), F32))
    return pl.pallas_call(
        body, name="rs_chip_add",
        grid_spec=pltpu.PrefetchScalarGridSpec(num_scalar_prefetch=1, grid=(2,), in_specs=in_specs,
                                               out_specs=out_specs),
        out_shape=_out_hbm(out_shape),
        compiler_params=pltpu.CompilerParams(vmem_limit_bytes=48 << 20))(jnp.stack([c, chip]), *_in_hbm(args))


def _rs_sibling_gather(reds, blks):
    n, k = len(reds), len(blks)
    vm = pl.BlockSpec(memory_space=pltpu.VMEM)

    def body(*refs):
        blk_in, red_out, blk_out = refs[n:n + k], refs[n + k:2 * n + k], refs[2 * n + k:2 * (n + k)]
        sems = refs[2 * (n + k):]
        _peer_barrier("both")
        start, _, end = _sibx_phases(red_out, *sems[:2])
        start()
        _gather_blocks(blk_in, blk_out, *sems[2:])
        end()

    res = pl.pallas_call(
        body, name="rs_sibling_gather",
        out_shape=[jax.ShapeDtypeStruct(r.shape, r.dtype) for r in reds]
        + [jax.ShapeDtypeStruct((8,) + b.shape, b.dtype) for b in blks],
        in_specs=[ANY] * n + [vm] * k, out_specs=[ANY] * n + [vm] * k,
        input_output_aliases={u: u for u in range(n)},
        scratch_shapes=[pltpu.SemaphoreType.DMA((n,))] * 2 + _gather_sems(k),
        compiler_params=pltpu.CompilerParams(collective_id=BARRIER_IDS["both"]))(*_in_hbm(reds), *blks)
    return res[:n], res[n:]


def _sibx_phases(o_refs, send_sems, recv_sems):
    n = len(o_refs)
    x, y, c = _place()
    cps = [pltpu.make_async_remote_copy(
        src_ref=o_refs[u].at[c], dst_ref=o_refs[u].at[c], send_sem=send_sems.at[u], recv_sem=recv_sems.at[u],
        device_id=(x, y, 1 - c), device_id_type=MESH) for u in range(n)]

    def start():
        for cp in cps:
            cp.start()

    def end():
        for u in range(n):
            cps[u].wait_send()
            pltpu.make_async_remote_copy(
                src_ref=o_refs[u].at[1 - c], dst_ref=o_refs[u].at[1 - c], send_sem=send_sems.at[u],
                recv_sem=recv_sems.at[u], device_id=(x, y, 1 - c), device_id_type=MESH).wait_recv()

    return start, None, end


def _carry_sibx(reds):
    n = len(reds)
    return _Carry(reds, [jax.ShapeDtypeStruct(r.shape, r.dtype) for r in reds], {u: u for u in range(n)},
                  [pltpu.SemaphoreType.DMA((n,))] * 2, lambda ins, outs, sems: _sibx_phases(outs, *sems), "sib")


def _prologue(blk, c_ctx, w, b, in8):
    n = w.shape[1]

    def body(blk_ref, cctx_ref, w_ref, b_ref, _in_in, g0_ref, c16_ref, g1_ref, in_ref, s1, r1, l1, s2, r2, s3, r3):
        _peer_barrier("both")
        x, y, c = _place()
        start, mid, end = _gather_phases([in_ref], s3, r3)
        start_mod, mid_mod, end_mod = _gather_phases([g1_ref], s2, r2)
        _gather_blocks([blk_ref], [g0_ref], s1, r1, l1)
        start()
        c16 = jnp.concatenate([g0_ref[i, 0:1, :] for i in range(8)] + [cctx_ref[...], jnp.zeros((7, D), F32)],
                              axis=0)
        c16_ref[...] = c16
        g1_ref[4 * x + 2 * y + c] = _dot(c16 * _sig(c16), w_ref[...], prec=HI) + b_ref[...]
        start_mod()
        mid()
        mid_mod()
        end()
        end_mod()

    vm = pl.BlockSpec(memory_space=pltpu.VMEM)
    return pl.pallas_call(
        body, name="prologue",
        out_shape=[jax.ShapeDtypeStruct((8, 8, D), F32), jax.ShapeDtypeStruct((16, D), F32),
                   jax.ShapeDtypeStruct((8, 16, n), F32), jax.ShapeDtypeStruct(in8.shape, in8.dtype)],
        in_specs=[vm, vm, vm, vm, ANY], out_specs=[vm, vm, vm, ANY], input_output_aliases={4: 3},
        scratch_shapes=_gather_sems(1) + [pltpu.SemaphoreType.DMA((1, 7))] * 4,
        compiler_params=pltpu.CompilerParams(vmem_limit_bytes=48 << 20,
                                             collective_id=BARRIER_IDS["both"]))(blk, c_ctx, w, b, in8)


def _ada_bwd(c16, dmod16, w, carry=None):
    n = w.shape[1]
    tn = 512

    def body(c_ref, d_ref, w_ref, gw_ref, gc_ref):
        j = pl.program_id(0)

        @pl.when(j == 0)
        def _():
            gc_ref[...] = jnp.zeros_like(gc_ref)

        cc = c_ref[...]
        dm = d_ref[...]
        gw_ref[...] = _dot(cc * _sig(cc), dm, TN, prec=HI)
        gc_ref[...] += _dot(dm, w_ref[...], NT, prec=HI)

    return _pcall(body, name="ada_bwd", grid=(n // tn,),
                  in_specs=[_full((16, D)), pl.BlockSpec((16, tn), lambda j: (0, j)),
                            pl.BlockSpec((D, tn), lambda j: (0, j))],
                  out_specs=[pl.BlockSpec((D, tn), lambda j: (0, j)), _full((16, D))],
                  out_shape=[jax.ShapeDtypeStruct((D, n), F32),
                             jax.ShapeDtypeStruct((16, D), F32)], carry=carry)(c16, dmod16, w)


def _adam_math(w, g, m, v):
    c1 = 1.0 - ADAM_B1 ** ADAM_STEP
    c2 = 1.0 - ADAM_B2 ** ADAM_STEP
    nm = ADAM_B1 * m + (1.0 - ADAM_B1) * g
    nv = ADAM_B2 * v + (1.0 - ADAM_B2) * (g * g)
    return -ADAM_LR * ((nm / c1) / (jnp.sqrt(nv / c2) + ADAM_EPS) + ADAM_WD * w), nm, nv


def _adamw_small(ws, gs, ms, vs):
    n = len(ws)

    def body(*refs):
        for u in range(n):
            d_, nm, nv = _adam_math(refs[u][...], refs[n + u][...], refs[2 * n + u][...], refs[3 * n + u][...])
            refs[4 * n + u][...] = d_
            refs[5 * n + u][...] = nm
            refs[6 * n + u][...] = nv

    specs = [_full(w.shape) for w in ws]
    shapes = [jax.ShapeDtypeStruct(w.shape, F32) for w in ws]
    out = _pcall(body, name="adamw_small", grid=(1,), in_specs=specs * 4, out_specs=specs * 3,
                 out_shape=shapes * 3)(*ws, *gs, *ms, *vs)
    return out[:n], out[n:2 * n], out[2 * n:]


def _cctx_grad(parts, c_ctx):
    def body(p_ref, c_ref, o_ref):
        acc = p_ref[0:1, :]
        for k in range(1, 4):
            acc = acc + p_ref[k:k + 1, :]
        cc = c_ref[...]
        s = _sig(cc)
        o_ref[...] = acc * (s * (1.0 + cc * (1.0 - s)))

    return _pcall(body, name="cctx_grad", grid=(1,), in_specs=[_full(parts.shape), _full((1, D))],
                  out_specs=_full((1, D)), out_shape=jax.ShapeDtypeStruct((1, D), F32))(parts, c_ctx)


ADAM_STEPS = 8


def _adamw_multi(ws, gs, ms, vs, *, name):
    n = len(ws)

    def body(*refs):
        for u in range(n):
            g = refs[n + u][...]
            refs[4 * n + u][...] = g
            refs[5 * n + u][...], refs[6 * n + u][...], refs[7 * n + u][...] = _adam_math(
                refs[u][...], g, refs[2 * n + u][...], refs[3 * n + u][...])

    specs = [pl.BlockSpec((w.shape[0] // ADAM_STEPS, w.shape[1]), lambda i: (i, 0)) for w in ws]
    shapes = [jax.ShapeDtypeStruct(w.shape, F32) for w in ws]
    out = _pcall(body, name=name, grid=(ADAM_STEPS,), in_specs=specs * 4, out_specs=specs * 4,
                 out_shape=shapes * 4)(*ws, *gs, *ms, *vs)
    return out[:n], out[n:2 * n], out[2 * n:3 * n], out[3 * n:]


def kernel(x, c, ctx, c_ctx, w_ada, b_ada, norm_mix_w, norm_ffn_w, w_in, hgrn_lb_logits, hgrn_norm_w, q_norm_w, k_norm_w, attn_sinks, w_branch_hgrn, w_branch_attn, w_out, w_ffn_gate, w_ffn_up, w_ffn_down, loss_target, m_c_ctx, m_w_ada, m_b_ada, m_norm_mix_w, m_norm_ffn_w, m_w_in, m_hgrn_lb_logits, m_hgrn_norm_w, m_q_norm_w, m_k_norm_w, m_attn_sinks, m_w_branch_hgrn, m_w_branch_attn, m_w_out, m_w_ffn_gate, m_w_ffn_up, m_w_ffn_down, v_c_ctx, v_w_ada, v_b_ada, v_norm_mix_w, v_norm_ffn_w, v_w_in, v_hgrn_lb_logits, v_hgrn_norm_w, v_q_norm_w, v_k_norm_w, v_attn_sinks, v_w_branch_hgrn, v_w_branch_attn, v_w_out, v_w_ffn_gate, v_w_ffn_up, v_w_ffn_down):
    xi, yi, ci = _place()
    chip = 2 * xi + yi
    dev = 2 * chip + ci
    s_len = x.shape[1]

    shards = [w_in[0].T, w_branch_hgrn[0], w_branch_attn[0], w_out[0], w_ffn_gate[0].T, w_ffn_up[0].T,
              w_ffn_down[0]]
    bufs = _cast_place(shards, ci, dev)

    lbrow = jnp.pad(hgrn_lb_logits.reshape(1, 512), ((0, 0), (0, D - 512)))
    blk = jnp.concatenate([c, lbrow, jnp.zeros((6, D), F32)], axis=0)
    nada = w_ada.shape[2]
    b_sh = lax.dynamic_slice(b_ada, (0, chip * nada), (1, nada))
    g0, c16, g1, in8 = _prologue(blk, c_ctx[None], w_ada[0], b_sh, bufs[0])
    lg = g0[0::2, 1, :512].reshape(4, 2, 2, 128).transpose(1, 2, 0, 3).reshape(2, 2, HGW)
    modall = g1[0::2].transpose(1, 0, 2).reshape(16, 4 * nada)
    mod = lax.dynamic_slice(modall, (dev, 0), (1, 6 * D)).reshape(6, D)
    modc = modall[8].reshape(6, D)[:2]

    sq, gx, _, small, rs = _local_step(
        x[0], ctx[0], loss_target[0], mod, modc, norm_mix_w, norm_ffn_w, lg, hgrn_norm_w, q_norm_w,
        k_norm_w, attn_sinks[0], in8.reshape(NCOL, D), bufs[1:], dist=(ci, chip))

    def whole(r):
        return r.reshape(2 * r.shape[1], r.shape[2])

    g_dn, g_g, g_u = [whole(r) for r in rs["ffn_done"]]
    g_bh, g_ba, g_o = [whole(r) for r in rs["mix_done"]]
    in_pairs = rs["in_pairs"]

    g2, tot = _ag_small(small["raw"], sq)
    loss = 0.5 * jnp.sum(tot[24]) / D
    dmodc_tot = jnp.pad(tot[6:8].reshape(1, 2 * D), ((0, 0), (0, 4 * D)))
    g_b_ada = tot[0:6].reshape(1, 6 * D) + dmodc_tot
    dmod16 = jnp.concatenate([g2[:, 0:6].reshape(8, 6 * D), dmodc_tot, jnp.zeros((7, 6 * D), F32)], axis=0)
    (g_w_ada, gc_part), in_contribs = _ada_bwd(
        c16, lax.dynamic_slice(dmod16, (0, chip * nada), (16, nada)), w_ada[0],
        carry=_carry_chipx(in_pairs, rows=rs["rest_rows"], into=rs["in_part"]))
    in_reds, (g3,) = _rs_sibling_gather(_rs_chip_add(in_pairs, in_contribs, ci, chip), [gc_part[8:16]])
    g_in = whole(in_reds[0])
    g_c_ctx = _cctx_grad(g3[0::2, 0], c_ctx[None])[0]
    g_nw1 = tot[8:9]
    g_nw2 = tot[9:10]
    g_hw = tot[10, :HGW].reshape(4, HGD).sum(0, keepdims=True)
    g_qnw = tot[10, HGW:].reshape(8, HDIM).sum(0, keepdims=True)
    g_knw = tot[11, :128].reshape(2, HDIM).sum(0, keepdims=True)
    g_sinks = tot[16:24, 0][None]
    g_lg = lax.dynamic_slice(tot[12:16, :HGW].reshape(2, 2, HGW), (0, 0, chip * 128), (2, 2, 128))

    names = ["c_ctx", "w_ada", "b_ada", "norm_mix_w", "norm_ffn_w", "w_in", "hgrn_lb_logits", "hgrn_norm_w",
             "q_norm_w", "k_norm_w", "attn_sinks", "w_branch_hgrn", "w_branch_attn", "w_out", "w_ffn_gate",
             "w_ffn_up", "w_ffn_down"]
    ws = dict(zip(names, [c_ctx, w_ada, b_ada, norm_mix_w, norm_ffn_w, w_in, hgrn_lb_logits, hgrn_norm_w,
                          q_norm_w, k_norm_w, attn_sinks, w_branch_hgrn, w_branch_attn, w_out, w_ffn_gate,
                          w_ffn_up, w_ffn_down]))
    ms = dict(zip(names, [m_c_ctx, m_w_ada, m_b_ada, m_norm_mix_w, m_norm_ffn_w, m_w_in, m_hgrn_lb_logits,
                          m_hgrn_norm_w, m_q_norm_w, m_k_norm_w, m_attn_sinks, m_w_branch_hgrn,
                          m_w_branch_attn, m_w_out, m_w_ffn_gate, m_w_ffn_up, m_w_ffn_down]))
    vs = dict(zip(names, [v_c_ctx, v_w_ada, v_b_ada, v_norm_mix_w, v_norm_ffn_w, v_w_in, v_hgrn_lb_logits,
                          v_hgrn_norm_w, v_q_norm_w, v_k_norm_w, v_attn_sinks, v_w_branch_hgrn,
                          v_w_branch_attn, v_w_out, v_w_ffn_gate, v_w_ffn_up, v_w_ffn_down]))
    transposed = ("w_in", "w_ffn_gate", "w_ffn_up")

    def view(a, n):
        return a[0].T if n in transposed else a[0]

    def unview(a, n):
        return a.T[None] if n in transposed else a[None]

    delta, new_m, new_v, grads = {}, {}, {}, {}

    def big_adamw(group, gs, name):
        g_, d_, m_, v_ = _adamw_multi([view(ws[n], n) for n in group], gs, [view(ms[n], n) for n in group],
                                      [view(vs[n], n) for n in group], name=name)
        for i, n in enumerate(group):
            grads[n], delta[n], new_m[n], new_v[n] = (unview(g_[i], n), unview(d_[i], n), unview(m_[i], n),
                                                      unview(v_[i], n))

    big_adamw(["w_ffn_down", "w_ffn_gate", "w_ffn_up", "w_out", "w_branch_hgrn", "w_branch_attn"],
              [g_dn, g_g, g_u, g_o, g_bh, g_ba], "adamw_first")
    big_adamw(["w_in", "w_ada"], [g_in, g_w_ada], "adamw_second")
    grads.update(c_ctx=g_c_ctx, b_ada=g_b_ada, norm_mix_w=g_nw1, norm_ffn_w=g_nw2, hgrn_lb_logits=g_lg,
                 hgrn_norm_w=g_hw, q_norm_w=g_qnw, k_norm_w=g_knw, attn_sinks=g_sinks)
    small_names = [n for n in names if n not in delta]

    def two_d(a):
        return a.reshape(1, -1) if a.ndim == 1 else a

    sd, sm_, sv = _adamw_small(*[[two_d(d[n]) for n in small_names] for d in (ws, grads, ms, vs)])
    for i, n in enumerate(small_names):
        for dst, src in ((delta, sd), (new_m, sm_), (new_v, sv)):
            dst[n] = src[i].reshape(ws[n].shape)
    return (loss, gx[None], *[grads[n] for n in names], *[delta[n] for n in names],
            *[new_m[n] for n in names], *[new_v[n] for n in names])
```

```python
import functools

import numpy as np
import jax
import jax.numpy as jnp
from jax import lax
from jax.experimental import pallas as pl
from jax.experimental.pallas import tpu as pltpu

F32 = jnp.float32
BF16 = jnp.bfloat16
HI = lax.Precision.HIGHEST
MESH = pl.DeviceIdType.MESH

D = 1024
L = 256
TM = 256
HGW = 512
HGD = 128
CH = 32
ATW = 512
HDIM = 64
BLK = 128
GRID_W = 64
DFF = 2816
NCOL = 5376
EPS = 1e-6
ROPE_THETA = 10000.0
BF16_SUBLANES = 16

C_FB, C_INP, C_QHG, C_FF = 0, 1, 2, 3
C_GATES = 1
C_GHG, C_QRAW = 8, 9
C_KV = 20
C_QKV = 6

ADAM_LR, ADAM_B1, ADAM_B2, ADAM_EPS, ADAM_WD, ADAM_STEP = 0.001, 0.9, 0.999, 1e-08, 0.01, 10

NN = (((1,), (0,)), ((), ()))
NT = (((1,), (1,)), ((), ()))
TN = (((0,), (0,)), ((), ()))


def _dot(a, b, dims=NN, prec=None):
    return lax.dot_general(a, b, dims, precision=prec, preferred_element_type=F32)


def _bdot(a, b, dims=NN):
    return _dot(a.astype(BF16), b.astype(BF16), dims)


def _sig(x):
    return 1.0 / (1.0 + jnp.exp(-x))


class _Carry:
    def __init__(self, ins, outs, aliases, scratch, phases, peers):
        self.ins, self.outs, self.aliases, self.scratch, self.phases = ins, outs, aliases, scratch, phases
        self.peers = peers


BARRIER_IDS = {"sib": 1, "chips": 2, "both": 3}


def _peers(kind):
    x, y, c = _place()
    peers = []
    if kind in ("sib", "both"):
        peers.append((x, y, 1 - c))
    if kind in ("chips", "both"):
        peers += [(1 - x, y, c), (x, 1 - y, c), (1 - x, 1 - y, c)]
    return peers


def _peer_signal(kind):
    for peer in _peers(kind):
        pl.semaphore_signal(pltpu.get_barrier_semaphore(), inc=1, device_id=peer, device_id_type=MESH)


def _peer_wait(kind):
    pl.semaphore_wait(pltpu.get_barrier_semaphore(), len(_peers(kind)))


def _peer_barrier(kind):
    _peer_signal(kind)
    _peer_wait(kind)


def _in_hbm(args):
    return [pltpu.with_memory_space_constraint(a, pltpu.HBM) for a in args]


def _out_hbm(shapes):
    if isinstance(shapes, (list, tuple)):
        return [pltpu.HBM(s.shape, s.dtype) for s in shapes]
    return pltpu.HBM(shapes.shape, shapes.dtype)


def _carry_join(a, b):
    na_in, na_out, na_sc = len(a.ins), len(a.outs), len(a.scratch)
    aliases = dict(a.aliases)
    aliases.update({na_in + i: na_out + o for i, o in b.aliases.items()})

    def phases(ins, outs, sems):
        pa = a.phases(ins[:na_in], outs[:na_out], sems[:na_sc])
        pb = b.phases(ins[na_in:], outs[na_out:], sems[na_sc:])

        def both(fa, fb):
            if fa is None and fb is None:
                return None

            def run():
                for fn in (fa, fb):
                    if fn is not None:
                        fn()
            return run

        return tuple(both(fa, fb) for fa, fb in zip(pa, pb))

    return _Carry(list(a.ins) + list(b.ins), list(a.outs) + list(b.outs), aliases,
                  list(a.scratch) + list(b.scratch), phases, a.peers if a.peers == b.peers else "both")


def _pcall(body, *, name, grid, in_specs, out_specs, out_shape, scratch=(), aliases=None, vmem_mb=48,
           carry=None):
    params = pltpu.CompilerParams(dimension_semantics=("arbitrary",) * len(grid),
                                  vmem_limit_bytes=vmem_mb << 20)
    if carry is None:
        plain = pl.pallas_call(
            body, name=name, grid=grid, in_specs=in_specs, out_specs=out_specs, out_shape=_out_hbm(out_shape),
            scratch_shapes=list(scratch), input_output_aliases=aliases or {}, compiler_params=params)
        return lambda *args: plain(*_in_hbm(args))
    single = not isinstance(out_shape, (list, tuple))
    out_specs_l = [out_specs] if single else list(out_specs)
    out_shape_l = [out_shape] if single else list(out_shape)
    n_in, n_out, n_sc = len(in_specs), len(out_shape_l), len(scratch)
    k_in, k_out = len(carry.ins), len(carry.outs)
    nsteps = int(np.prod(grid))
    assert nsteps >= 3

    def wrapped(*refs):
        ins, cins = refs[:n_in], refs[n_in:n_in + k_in]
        o0 = n_in + k_in
        outs, couts = refs[o0:o0 + n_out], refs[o0 + n_out:o0 + n_out + k_out]
        s0 = o0 + n_out + k_out
        sc, csc = refs[s0:s0 + n_sc], refs[s0 + n_sc:]
        step = pl.program_id(0)
        for ax in range(1, len(grid)):
            step = step * grid[ax] + pl.program_id(ax)
        start, mid, end = carry.phases(cins, couts, csc)

        pl.when(step == 0)(functools.partial(_peer_signal, carry.peers))
        body(*ins, *outs, *sc)

        @pl.when(step == 0)
        def _():
            _peer_wait(carry.peers)
            start()

        if mid is not None:
            pl.when(step == nsteps - 2)(mid)
        pl.when(step == nsteps - 1)(end)

    all_aliases = dict(aliases or {})
    all_aliases.update({n_in + i: n_out + o for i, o in carry.aliases.items()})
    call = pl.pallas_call(
        wrapped, name=name, grid=grid, in_specs=list(in_specs) + [ANY] * k_in,
        out_specs=out_specs_l + [ANY] * k_out, out_shape=_out_hbm(out_shape_l + list(carry.outs)),
        scratch_shapes=list(scratch) + list(carry.scratch), input_output_aliases=all_aliases,
        compiler_params=pltpu.CompilerParams(dimension_semantics=("arbitrary",) * len(grid),
                                             vmem_limit_bytes=vmem_mb << 20,
                                             collective_id=BARRIER_IDS[carry.peers]))

    def run(*args):
        res = call(*_in_hbm(args), *_in_hbm(carry.ins))
        core = res[:n_out]
        return (core[0] if single else list(core)), list(res[n_out:])

    return run


def _full(shape):
    nd = len(shape)
    return pl.BlockSpec(shape, lambda *_: (0,) * nd)


ANY = pl.BlockSpec(memory_space=pl.ANY)


NT_IN = NCOL // 256


def _src_block(j):
    return j + jnp.where(j < 4, 2, jnp.where(j < 6, 3, jnp.where(j < 8, -6, jnp.where(
        j < 16, 5, jnp.where(j < 20, -7, -14)))))


def _mm_in(h, wt, tm, carry=None):
    tt = h.shape[0]

    def body(h_ref, w_ref, o_ref):
        o_ref[...] = _bdot(h_ref[...], w_ref[...], NT)

    return _pcall(body, name="mm_in", grid=(tt // tm, NT_IN),
                  in_specs=[pl.BlockSpec((tm, D), lambda i, j: (i, 0)),
                            pl.BlockSpec((256, D), lambda i, j: (_src_block(j), 0))],
                  out_specs=pl.BlockSpec((tm, 256), lambda i, j: (i, j)),
                  out_shape=jax.ShapeDtypeStruct((tt, NCOL), F32), carry=carry)(h, wt)


def _mm_dh(dp, wt, tm, carry=None):
    tt = dp.shape[0]
    per, ng = 3, NT_IN // 3

    def body(d_ref, w0, w1, w2, o_ref, acc):
        kk = pl.program_id(1)

        @pl.when(kk == 0)
        def _():
            acc[...] = jnp.zeros_like(acc)

        acc[...] += (_bdot(d_ref[:, 0:256], w0[...]) + _bdot(d_ref[:, 256:512], w1[...])
                     + _bdot(d_ref[:, 512:768], w2[...]))

        @pl.when(kk == ng - 1)
        def _():
            o_ref[...] = acc[...]

    wspecs = [pl.BlockSpec((256, D), functools.partial(lambda t, i, kk: (_src_block(per * kk + t), 0), t))
              for t in range(per)]
    return _pcall(body, name="mm_dh", grid=(tt // tm, ng),
                  in_specs=[pl.BlockSpec((tm, per * 256), lambda i, kk: (i, kk))] + wspecs,
                  out_specs=pl.BlockSpec((tm, D), lambda i, kk: (i, 0)),
                  out_shape=jax.ShapeDtypeStruct((tt, D), F32), scratch=[pltpu.VMEM((tm, D), F32)],
                  carry=carry)(dp, wt, wt, wt)


def _mm_gin(dp, h, tk):
    tt = dp.shape[0]
    nk = tt // tk

    def body(d_ref, h_ref, o_ref, acc):
        kk = pl.program_id(1)

        @pl.when(kk == 0)
        def _():
            acc[...] = jnp.zeros_like(acc)

        acc[...] += _bdot(d_ref[...], h_ref[...], TN)

        @pl.when(kk == nk - 1)
        def _():
            o_ref[...] = acc[...].astype(BF16)

    return _pcall(body, name="mm_gin", grid=(NT_IN, nk),
                  in_specs=[pl.BlockSpec((tk, 256), lambda j, kk: (kk, j)),
                            pl.BlockSpec((tk, D), lambda j, kk: (kk, 0))],
                  out_specs=pl.BlockSpec((256, D), lambda j, kk: (_src_block(j), 0)),
                  out_shape=jax.ShapeDtypeStruct((NCOL, D), BF16), scratch=[pltpu.VMEM((256, D), F32)])(dp, h)


def _tok_specs():
    assert L == TM
    return [_full((TM, D)), pl.BlockSpec((TM, D), lambda i: (jnp.maximum(i - 1, 0), 0))]


def _mod1(ctx, x, nw, ss):
    rows = L + x.shape[0]

    def body(c_ref, x_ref, nw_ref, ss_ref, h_ref):
        t = jnp.where(pl.program_id(0) == 0, c_ref[...], x_ref[...])
        r = lax.rsqrt(jnp.mean(t * t, axis=-1, keepdims=True) + EPS)
        s = ss_ref[0]
        h_ref[...] = ((t * r * nw_ref[...]) * (1.0 + s[1:2]) + s[0:1]).astype(BF16)

    return _pcall(body, name="mod1", grid=(rows // TM,),
                  in_specs=_tok_specs() + [_full((1, D)),
                                           pl.BlockSpec((1, 2, D), lambda i: (jnp.minimum(i, 1), 0, 0))],
                  out_specs=pl.BlockSpec((TM, D), lambda i: (i, 0)),
                  out_shape=jax.ShapeDtypeStruct((rows, D), BF16))(ctx, x, nw, ss)


def _norm_bwd_rows(x, dh, nw, scale):
    r = lax.rsqrt(jnp.mean(x * x, axis=-1, keepdims=True) + EPS)
    xh = x * r
    dxh = dh * ((1.0 + scale) * nw)
    dx = r * (dxh - xh * jnp.mean(dxh * xh, axis=-1, keepdims=True))
    return dx, xh


def _out_proj_mod2(mixed, w_o, x, g1, nw2, ss2):
    s_len = x.shape[0]
    tm = 512

    def body(m_ref, w_ref, x_ref, g_ref, nw_ref, ss_ref, ao_ref, x1_ref, h_ref):
        ao = _bdot(m_ref[...], w_ref[...])
        ao_ref[...] = ao.astype(BF16)
        x1 = x_ref[...] + g_ref[...] * ao
        x1_ref[...] = x1
        r = lax.rsqrt(jnp.mean(x1 * x1, axis=-1, keepdims=True) + EPS)
        s = ss_ref[0]
        h_ref[...] = ((x1 * r * nw_ref[...]) * (1.0 + s[1:2]) + s[0:1]).astype(BF16)

    row = pl.BlockSpec((tm, D), lambda i: (i, 0))
    f = jax.ShapeDtypeStruct((s_len, D), F32)
    return _pcall(body, name="out_proj_mod2", grid=(s_len // tm,),
                  in_specs=[row, _full((D, D)), row, _full((1, D)), _full((1, D)), _full((1, 2, D))],
                  out_specs=[row, row, row],
                  out_shape=[jax.ShapeDtypeStruct((s_len, D), BF16), f,
                             jax.ShapeDtypeStruct((s_len, D), BF16)])(mixed, w_o, x, g1, nw2, ss2)


TS = 1024


def _acc_call(body, *, name, grid, in_specs, out_specs, out_shape, acc_shapes, args, carry=None):
    return _pcall(body, name=name, grid=grid, in_specs=in_specs, out_specs=out_specs, out_shape=out_shape,
                  scratch=[pltpu.VMEM(s, F32) for s in acc_shapes], carry=carry)(*args)


def _ffn_up(h2, g4, u4, carry=None):
    s_len = h2.shape[0]
    ns = g4.shape[1]

    def body(h_ref, g_ref, u_ref, a_ref, b_ref, z_ref):
        h = h_ref[...]
        a = _bdot(h, g_ref[0], NT)
        b = _bdot(h, u_ref[0], NT)
        a_ref[0] = a.astype(BF16)
        b_ref[0] = b.astype(BF16)
        z_ref[0] = (a * _sig(a) * b).astype(BF16)

    w = pl.BlockSpec((1, ns, D), lambda i, j: (j, 0, 0))
    o = pl.BlockSpec((1, TS, ns), lambda i, j: (j, i, 0))
    f = jax.ShapeDtypeStruct((4, s_len, ns), BF16)
    return _pcall(body, name="ffn_up", grid=(s_len // TS, 4),
                  in_specs=[pl.BlockSpec((TS, D), lambda i, j: (i, 0)), w, w], out_specs=[o, o, o],
                  out_shape=[f, f, jax.ShapeDtypeStruct((4, s_len, ns), BF16)], carry=carry)(h2, g4, u4)


def _ffn_down_loss(z4, dn4, x1, g2, tgt):
    _, s_len, ns = z4.shape

    def body(z_ref, w_ref, x1_ref, g_ref, t_ref, sq_ref, dx2_ref, dyb_ref, dg_ref, acc):
        i, j = pl.program_id(0), pl.program_id(1)

        @pl.when((i == 0) & (j == 0))
        def _():
            sq_ref[...] = jnp.zeros_like(sq_ref)
            dg_ref[...] = jnp.zeros_like(dg_ref)

        @pl.when(j == 0)
        def _():
            acc[...] = jnp.zeros_like(acc)

        acc[...] += _bdot(z_ref[0], w_ref[0])

        @pl.when(j == 3)
        def _():
            y_ = acc[...]
            g = g_ref[...]
            e = x1_ref[...] + g * y_ - t_ref[...]
            sq_ref[...] += jnp.sum(e * e, axis=0, keepdims=True)
            dx2 = e * (1.0 / D)
            dx2_ref[...] = dx2
            dyb_ref[...] = (g * dx2).astype(BF16)
            dg_ref[...] += jnp.sum(dx2 * y_, axis=0, keepdims=True)

    row = pl.BlockSpec((TS, D), lambda i, j: (i, 0))
    vec = _full((1, D))
    return _acc_call(body, name="ffn_down_loss", grid=(s_len // TS, 4),
                     in_specs=[pl.BlockSpec((1, TS, ns), lambda i, j: (j, i, 0)),
                               pl.BlockSpec((1, ns, D), lambda i, j: (j, 0, 0)), row, vec, row],
                     out_specs=[vec, row, row, vec],
                     out_shape=[jax.ShapeDtypeStruct((1, D), F32), jax.ShapeDtypeStruct((s_len, D), F32),
                                jax.ShapeDtypeStruct((s_len, D), BF16), jax.ShapeDtypeStruct((1, D), F32)],
                     acc_shapes=[(TS, D)], args=(z4, dn4, x1, g2, tgt))


def _ffn_dz(dyb, dn4, a4, b4):
    _, s_len, ns = a4.shape

    def body(dy_ref, w_ref, a_ref, b_ref, da_ref, db_ref):
        dz = _bdot(dy_ref[...], w_ref[0], NT)
        a = a_ref[0].astype(F32)
        s = _sig(a)
        da_ref[0] = (dz * b_ref[0].astype(F32) * (s * (1.0 + a * (1.0 - s)))).astype(BF16)
        db_ref[0] = (dz * (a * s)).astype(BF16)

    t = pl.BlockSpec((1, TS, ns), lambda i, j: (j, i, 0))
    o = jax.ShapeDtypeStruct((4, s_len, ns), BF16)
    return _pcall(body, name="ffn_dz", grid=(s_len // TS, 4),
                  in_specs=[pl.BlockSpec((TS, D), lambda i, j: (i, 0)),
                            pl.BlockSpec((1, ns, D), lambda i, j: (j, 0, 0)), t, t],
                  out_specs=[t, t], out_shape=[o, o])(dyb, dn4, a4, b4)


def _ffn_gdn(z4, dyb):
    _, s_len, ns = z4.shape
    tk = min(s_len, 2 * TS)
    nk = s_len // tk

    def body(z_ref, dy_ref, o_ref, acc):
        t = pl.program_id(1)

        @pl.when(t == 0)
        def _():
            acc[...] = jnp.zeros_like(acc)

        acc[...] += _bdot(z_ref[0], dy_ref[...], TN)

        @pl.when(t == nk - 1)
        def _():
            o_ref[0] = acc[...].astype(o_ref.dtype)

    return _acc_call(body, name="ffn_gdn", grid=(4, nk),
                     in_specs=[pl.BlockSpec((1, tk, ns), lambda j, t: (j, t, 0)),
                               pl.BlockSpec((tk, D), lambda j, t: (t, 0))],
                     out_specs=pl.BlockSpec((1, ns, D), lambda j, t: (j, 0, 0)),
                     out_shape=jax.ShapeDtypeStruct((4, ns, D), BF16), acc_shapes=[(ns, D)], args=(z4, dyb))


def _ffn_dh2(da4, db4, g4, u4, carry=None):
    _, s_len, ns = da4.shape

    def body(da_ref, db_ref, g_ref, u_ref, o_ref, acc):
        j = pl.program_id(1)

        @pl.when(j == 0)
        def _():
            acc[...] = jnp.zeros_like(acc)

        acc[...] += _bdot(da_ref[0], g_ref[0]) + _bdot(db_ref[0], u_ref[0])

        @pl.when(j == 3)
        def _():
            o_ref[...] = acc[...]

    t = pl.BlockSpec((1, TS, ns), lambda i, j: (j, i, 0))
    w = pl.BlockSpec((1, ns, D), lambda i, j: (j, 0, 0))
    return _acc_call(body, name="ffn_dh2", grid=(s_len // TS, 4), in_specs=[t, t, w, w],
                     out_specs=pl.BlockSpec((TS, D), lambda i, j: (i, 0)),
                     out_shape=jax.ShapeDtypeStruct((s_len, D), F32), acc_shapes=[(TS, D)],
                     args=(da4, db4, g4, u4), carry=carry)


def _ffn_ggu(h2, da4, db4, carry=None):
    _, s_len, ns = da4.shape
    nk = s_len // TS

    def body(h_ref, da_ref, db_ref, gg_ref, gu_ref, acc_g, acc_u):
        t = pl.program_id(1)

        @pl.when(t == 0)
        def _():
            acc_g[...] = jnp.zeros_like(acc_g)
            acc_u[...] = jnp.zeros_like(acc_u)

        h = h_ref[...]
        acc_g[...] += _bdot(da_ref[0], h, TN)
        acc_u[...] += _bdot(db_ref[0], h, TN)

        @pl.when(t == nk - 1)
        def _():
            gg_ref[0] = acc_g[...].astype(BF16)
            gu_ref[0] = acc_u[...].astype(BF16)

    d = pl.BlockSpec((1, TS, ns), lambda j, t: (j, t, 0))
    o = pl.BlockSpec((1, ns, D), lambda j, t: (j, 0, 0))
    f = jax.ShapeDtypeStruct((4, ns, D), BF16)
    return _acc_call(body, name="ffn_ggu", grid=(4, nk),
                     in_specs=[pl.BlockSpec((TS, D), lambda j, t: (t, 0)), d, d], out_specs=[o, o],
                     out_shape=[f, f], acc_shapes=[(ns, D), (ns, D)], args=(h2, da4, db4), carry=carry)


def _mod2_bwd(x1, dh2, dx2, ao, nw2, ss2, g1):
    s_len = x1.shape[0]

    def body(x1_ref, dh_ref, dx2_ref, ao_ref, nw_ref, ss_ref, g_ref,
             dx1_ref, da_ref, dss_ref, dnw_ref, dg_ref):
        i = pl.program_id(0)

        @pl.when(i == 0)
        def _():
            dss_ref[...] = jnp.zeros_like(dss_ref)
            dnw_ref[...] = jnp.zeros_like(dnw_ref)
            dg_ref[...] = jnp.zeros_like(dg_ref)

        dh = dh_ref[...]
        nw = nw_ref[...]
        scale = ss_ref[0][1:2]
        dxn, xh = _norm_bwd_rows(x1_ref[...], dh, nw, scale)
        dx1 = dx2_ref[...] + dxn
        dx1_ref[...] = dx1
        da_ref[...] = (g_ref[...] * dx1).astype(BF16)
        dg_ref[...] += jnp.sum(dx1 * ao_ref[...].astype(F32), axis=0, keepdims=True)
        dsh = jnp.sum(dh, axis=0, keepdims=True)
        dsc = jnp.sum(dh * xh * nw, axis=0, keepdims=True)
        dss_ref[...] += jnp.concatenate([dsh, dsc], axis=0)
        dnw_ref[...] += jnp.sum(dh * xh * (1.0 + scale), axis=0, keepdims=True)

    row = pl.BlockSpec((TM, D), lambda i: (i, 0))
    vec = _full((1, D))
    return _pcall(body, name="mod2_bwd", grid=(s_len // TM,),
                  in_specs=[row, row, row, row, vec, _full((1, 2, D)), vec],
                  out_specs=[row, row, _full((2, D)), vec, vec],
                  out_shape=[jax.ShapeDtypeStruct((s_len, D), F32), jax.ShapeDtypeStruct((s_len, D), BF16),
                             jax.ShapeDtypeStruct((2, D), F32), jax.ShapeDtypeStruct((1, D), F32),
                             jax.ShapeDtypeStruct((1, D), F32)])(x1, dh2, dx2, ao, nw2, ss2, g1)


def _mod1_bwd(ctx, x, dh, dx1, nw1, ss1):
    s_len = dx1.shape[0]
    tt = L + s_len

    def body(c_ref, x_ref, dh_ref, dx1_ref, nw_ref, ss_ref, dx_ref, dss_ref, dnw_ref):
        i = pl.program_id(0)
        tok = jnp.where(i == 0, c_ref[...], x_ref[...])

        @pl.when(i == 0)
        def _():
            dnw_ref[...] = jnp.zeros_like(dnw_ref)

        @pl.when(i <= 1)
        def _():
            dss_ref[...] = jnp.zeros_like(dss_ref)

        dh_ = dh_ref[...]
        nw = nw_ref[...]
        scale = ss_ref[0][1:2]
        dxn, xh = _norm_bwd_rows(tok, dh_, nw, scale)

        @pl.when(i >= 1)
        def _():
            dx_ref[...] = dx1_ref[...] + dxn

        dsh = jnp.sum(dh_, axis=0, keepdims=True)
        dsc = jnp.sum(dh_ * xh * nw, axis=0, keepdims=True)
        dss_ref[...] += jnp.concatenate([dsh, dsc], axis=0)[None]
        dnw_ref[...] += jnp.sum(dh_ * xh * (1.0 + scale), axis=0, keepdims=True)

    row = pl.BlockSpec((TM, D), lambda i: (i, 0))
    lat = pl.BlockSpec((TM, D), lambda i: (jnp.maximum(i - 1, 0), 0))
    sel = pl.BlockSpec((1, 2, D), lambda i: (jnp.minimum(i, 1), 0, 0))
    return _pcall(body, name="mod1_bwd", grid=(tt // TM,),
                  in_specs=_tok_specs() + [row, lat, _full((1, D)), sel],
                  out_specs=[lat, sel, _full((1, D))],
                  out_shape=[jax.ShapeDtypeStruct((s_len, D), F32), jax.ShapeDtypeStruct((2, 2, D), F32),
                             jax.ShapeDtypeStruct((1, D), F32)])(ctx, x, dh, dx1, nw1, ss1)


def _rows(c):
    return slice(c * CH, (c + 1) * CH)


def _chunk_masks(rev, transpose=False):
    r = lax.broadcasted_iota(jnp.int32, (TM, TM), 0)
    c = lax.broadcasted_iota(jnp.int32, (TM, TM), 1)
    same = (r // CH) == (c // CH)
    before = (c >= r) if (rev != transpose) else (c <= r)
    return same & before, same


def _chunk_scan(x, rev, transpose=False):
    r = lax.broadcasted_iota(jnp.int32, (CH, CH), 0)
    c = lax.broadcasted_iota(jnp.int32, (CH, CH), 1)
    tri = ((c >= r) if (rev != transpose) else (c <= r)).astype(F32)
    return jnp.concatenate([_dot(tri, x[_rows(ch)], prec=HI) for ch in range(x.shape[0] // CH)], axis=0)


def _chunk_total(x):
    return jnp.concatenate([jnp.broadcast_to(jnp.sum(x[_rows(ch)], axis=0, keepdims=True), (CH, x.shape[1]))
                            for ch in range(x.shape[0] // CH)], axis=0)


def _hgrn_gate(fl, qraw, lg):
    lb = 1.0 / (1.0 + jnp.exp(lg[1:2] - lg[0:1]))
    sg = _sig(fl)
    f = lb + (1.0 - lb) * sg
    q = qraw * _sig(qraw) * (HGD ** -0.5)
    return lb, sg, f, q


def _hgrn_fwd(p, lg, *, rev, carry=None, readout=None):
    tt = p.shape[0]
    nt = tt // TM
    ncht = TM // CH
    d = 1 if rev else 0

    def tile_of(s):
        return jnp.where(s == 0, 0, nt - s) if rev else s

    def body(*refs):
        if readout is None:
            f_ref, inp_ref, q_ref, lg_ref, o_ref, st_ref, state = refs
        else:
            f_ref, inp_ref, q_ref, lg_ref, oo_ref, g_ref, hw_ref, o_ref, st_ref, y_ref, state = refs
        s = pl.program_id(0)

        @pl.when(s == 0)
        def _():
            state[...] = jnp.zeros_like(state)

        _, _, f, q = _hgrn_gate(f_ref[...], q_ref[...], lg_ref[0])
        lf = jnp.log(f)
        causal, _ = _chunk_masks(rev)
        cum = _chunk_scan(lf, rev)
        tot = _chunk_total(lf)
        qd = (q * jnp.exp(cum)).astype(BF16)
        kd = ((1.0 - f) * jnp.exp(-cum)).astype(BF16)
        ke = ((1.0 - f) * jnp.exp(tot - cum)).astype(BF16)
        et = jnp.exp(tot)
        v = inp_ref[...].astype(BF16)
        order = range(ncht - 1, -1, -1) if rev else range(ncht)
        outs = []
        for h in range(4):
            sl = slice(h * HGD, (h + 1) * HGD)
            qd_, kd_, ke_, v_ = qd[:, sl], kd[:, sl], ke[:, sl], v[:, sl]
            pm = jnp.where(causal, _dot(qd_, kd_, NT), 0.0).astype(BF16)
            o_h = _dot(pm, v_)
            upd = [_dot(v_[_rows(c)], ke_[_rows(c)], TN) for c in range(ncht)]
            st = state[h]
            for c in order:
                st_ref[c, h] = st
                st = st * et[c * CH:c * CH + 1, sl] + upd[c]
            state[h] = st
            inter = [_dot(qd_[_rows(c)], st_ref[c, h].astype(BF16), NT) for c in range(ncht)]
            outs.append(o_h + jnp.concatenate(inter, axis=0))
        o_tile = jnp.concatenate(outs, axis=1)
        o_ref[...] = o_tile
        if readout is not None:
            @pl.when(tile_of(s) >= 1)
            def _():
                g = g_ref[...]
                y_ref[...] = (_head_rms(oo_ref[...] + o_tile, None, 4) * hw_ref[...] * (g * _sig(g))).astype(BF16)

    def col(cb):
        return pl.BlockSpec((TM, HGW), lambda s: (tile_of(s), cb))

    in_specs = [col(C_FB if rev else C_FF), col(C_INP), col(C_QHG), pl.BlockSpec((1, 2, HGW), lambda s: (d, 0, 0))]
    out_specs = [col(0), pl.BlockSpec((ncht, 4, HGD, HGD), lambda s: (tile_of(s), 0, 0, 0))]
    out_shape = [jax.ShapeDtypeStruct((tt, HGW), F32), jax.ShapeDtypeStruct((nt * ncht, 4, HGD, HGD), F32)]
    args = [p, p, p, lg]
    if readout is not None:
        in_specs += [col(0), col(C_GHG), _full((1, HGW))]
        args += [readout[0], p, readout[1]]
        assert rev
        out_specs.append(pl.BlockSpec((TM, HGW), lambda s: (jnp.where(s == 0, nt - 2, tile_of(s) - 1), 0)))
        out_shape.append(jax.ShapeDtypeStruct((tt - L, HGW), BF16))
    return _pcall(body, name="hgrn_fwd_rev" if rev else "hgrn_fwd", grid=(nt,), in_specs=in_specs,
                  out_specs=out_specs, out_shape=out_shape, scratch=[pltpu.VMEM((4, HGD, HGD), F32)],
                  carry=carry)(*args)


def _hgrn_bwd(p, lg, do, st, dp, prev, *, rev, carry=None):
    tt = p.shape[0]
    nt = tt // TM
    ncht = TM // CH
    d = 1 if rev else 0
    second = prev is not None

    def tile_of(s):
        return jnp.where(s == nt - 1, 0, s + 1) if rev else nt - 1 - s

    def body(*refs):
        if second:
            (f_ref, inp_ref, q_ref, lg_ref, do_ref, st_ref, dvp_ref, dqp_ref, _dp_in,
             dp_ref, dlg_ref, dstate) = refs
        else:
            (f_ref, inp_ref, q_ref, lg_ref, do_ref, st_ref, _dp_in,
             dp_ref, dv_ref, dq_ref, dlg_ref, dstate) = refs
        s = pl.program_id(0)
        tile = tile_of(s)

        @pl.when(s == 0)
        def _():
            dstate[...] = jnp.zeros_like(dstate)
            dlg_ref[...] = jnp.zeros_like(dlg_ref)

        qraw = q_ref[...]
        lb, sg, f, q = _hgrn_gate(f_ref[...], qraw, lg_ref[0])
        lf = jnp.log(f)
        causal, _ = _chunk_masks(rev)
        causal_t, _ = _chunk_masks(rev, transpose=True)
        cum = _chunk_scan(lf, rev)
        tot = _chunk_total(lf)
        ea, eb, ee, et = jnp.exp(cum), jnp.exp(-cum), jnp.exp(tot - cum), jnp.exp(tot)
        qdf, kdf, kef = q * ea, (1.0 - f) * eb, (1.0 - f) * ee
        qd, kd, ke = qdf.astype(BF16), kdf.astype(BF16), kef.astype(BF16)
        v = inp_ref[...].astype(BF16)
        dob = jnp.where(tile == 0, 0.0, do_ref[...]).astype(BF16)
        order = range(ncht) if rev else range(ncht - 1, -1, -1)
        dq_l, dk_l, dv_l, dcum_l, dtot_l = [], [], [], [], []
        for h in range(4):
            sl = slice(h * HGD, (h + 1) * HGD)
            qd_, kd_, ke_, v_, do_ = qd[:, sl], kd[:, sl], ke[:, sl], v[:, sl], dob[:, sl]
            pmt = jnp.where(causal_t, _dot(kd_, qd_, NT), 0.0).astype(BF16)
            dpm = jnp.where(causal, _dot(do_, v_, NT), 0.0).astype(BF16)
            dpmt = jnp.where(causal_t, _dot(v_, do_, NT), 0.0).astype(BF16)
            dv = _dot(pmt, do_)
            dqd = _dot(dpm, kd_)
            dkd = _dot(dpmt, qd_)
            upd = [_dot(do_[_rows(c)], qd_[_rows(c)], TN) for c in range(ncht)]
            ds = dstate[h]
            ds1 = [None] * ncht
            for c in order:
                ds1[c] = ds
                ds = ds * et[c * CH:c * CH + 1, sl] + upd[c]
            dstate[h] = ds
            dke_c, dv_c, dqd_c, dtot_c = [], [], [], []
            for c in range(ncht):
                st0 = st_ref[c, h]
                dsb = ds1[c].astype(BF16)
                dke_ = _dot(v_[_rows(c)], dsb)
                dke_c.append(dke_)
                dv_c.append(_dot(ke_[_rows(c)], dsb, NT))
                dqd_c.append(_dot(do_[_rows(c)], st0.astype(BF16)))
                dt = (jnp.sum(ds1[c] * st0, axis=0, keepdims=True) * et[c * CH:c * CH + 1, sl]
                      + jnp.sum(dke_ * kef[_rows(c), sl], axis=0, keepdims=True))
                dtot_c.append(jnp.broadcast_to(dt, (CH, HGD)))
            dke = jnp.concatenate(dke_c, axis=0)
            dqd = dqd + jnp.concatenate(dqd_c, axis=0)
            dv_l.append(dv + jnp.concatenate(dv_c, axis=0))
            dtot_l.append(jnp.concatenate(dtot_c, axis=0))
            dq_l.append(dqd * ea[:, sl])
            dk_l.append(dkd * eb[:, sl] + dke * ee[:, sl])
            dcum_l.append(dqd * qdf[:, sl] - dkd * kdf[:, sl] - dke * kef[:, sl])
        dcum = jnp.concatenate(dcum_l, axis=1)
        dlf = _chunk_scan(dcum, rev, transpose=True) + jnp.concatenate(dtot_l, axis=1)
        dq_t = jnp.concatenate(dq_l, axis=1)
        dv_t = jnp.concatenate(dv_l, axis=1)

        df = dlf / f - jnp.concatenate(dk_l, axis=1)
        dfl = df * (1.0 - lb) * sg * (1.0 - sg)
        dlb = jnp.sum(df * (1.0 - sg), axis=0, keepdims=True)
        dl0 = dlb * lb * (1.0 - lb)
        dlg_ref[...] += jnp.concatenate([dl0, -dl0], axis=0)[None]
        if second:
            sq = _sig(qraw)
            dqr = (dqp_ref[...] + dq_t) * (HGD ** -0.5) * (sq * (1.0 + qraw * (1.0 - sq)))
            dp_ref[...] = jnp.concatenate([dfl, dvp_ref[...] + dv_t, dqr], axis=1).astype(BF16)
        else:
            dp_ref[...] = dfl.astype(BF16)
            dv_ref[...] = dv_t
            dq_ref[...] = dq_t

    def col(cb):
        return pl.BlockSpec((TM, HGW), lambda s: (tile_of(s), cb))

    tok = pl.BlockSpec((TM, HGW), lambda s: (tile_of(s), 0))
    in_specs = [col(C_FB if rev else C_FF), col(C_INP), col(C_QHG),
                pl.BlockSpec((1, 2, HGW), lambda s: (d, 0, 0)),
                pl.BlockSpec((TM, HGW), lambda s: (jnp.maximum(tile_of(s) - 1, 0), 0)),
                pl.BlockSpec((ncht, 4, HGD, HGD), lambda s: (tile_of(s), 0, 0, 0))]
    args = [p, p, p, lg, do, st]
    dlg_spec = _full((1, 2, HGW))
    dlg_shape = jax.ShapeDtypeStruct((1, 2, HGW), F32)
    if second:
        in_specs += [tok, tok]
        args += [prev[0], prev[1]]
        out_specs = [pl.BlockSpec((TM, 3 * HGW), lambda s: (tile_of(s), 0)), dlg_spec]
        out_shape = [jax.ShapeDtypeStruct(dp.shape, BF16), dlg_shape]
    else:
        out_specs = [pl.BlockSpec((TM, HGW), lambda s: (tile_of(s), C_FB if rev else C_FF)), tok, tok, dlg_spec]
        out_shape = [jax.ShapeDtypeStruct(dp.shape, BF16), jax.ShapeDtypeStruct((tt, HGW), F32),
                     jax.ShapeDtypeStruct((tt, HGW), F32), dlg_shape]
    in_specs.append(ANY)
    args.append(dp)
    return _pcall(body, name="hgrn_bwd_rev" if rev else "hgrn_bwd", grid=(nt,),
                  in_specs=in_specs, out_specs=out_specs, out_shape=out_shape,
                  scratch=[pltpu.VMEM((4, HGD, HGD), F32)],
                  aliases={len(args) - 1: 0}, carry=carry)(*args)


def _head_rms(o, w, nheads):
    outs = []
    for h in range(nheads):
        oh = o[:, h * HGD:(h + 1) * HGD]
        outs.append(oh * lax.rsqrt(jnp.mean(oh * oh, axis=-1, keepdims=True) + EPS))
    return jnp.concatenate(outs, axis=1)


def _readout_bwd(o0, o1, p, hw4, dy, dp, carry=None):
    tt = o0.shape[0]
    s_len = tt - L

    def body(o0_ref, o1_ref, g_ref, w_ref, dy_ref, _dp_in, dp_ref, do_ref, dw_ref):
        i = pl.program_id(0)

        @pl.when(i == 0)
        def _():
            dw_ref[...] = jnp.zeros_like(dw_ref)
            dp_ref[...] = jnp.zeros_like(dp_ref)

        @pl.when(i >= 1)
        def _():
            o = o0_ref[...] + o1_ref[...]
            g = g_ref[...]
            w = w_ref[...]
            sg = _sig(g)
            dy_ = dy_ref[...]
            dsw = dy_ * (g * sg)
            outs, xhs = [], []
            for h in range(4):
                sl = slice(h * HGD, (h + 1) * HGD)
                oh = o[:, sl]
                r = lax.rsqrt(jnp.mean(oh * oh, axis=-1, keepdims=True) + EPS)
                xh = oh * r
                dxh = dsw[:, sl] * w[:, sl]
                outs.append(r * (dxh - xh * jnp.mean(dxh * xh, axis=-1, keepdims=True)))
                xhs.append(xh)
            xh = jnp.concatenate(xhs, axis=1)
            do_ref[...] = jnp.concatenate(outs, axis=1)
            dp_ref[...] = (dy_ * xh * w * (sg * (1.0 + g * (1.0 - sg)))).astype(BF16)
            dw_ref[...] += jnp.sum(dsw * xh, axis=0, keepdims=True)

    tok = pl.BlockSpec((TM, HGW), lambda i: (i, 0))
    lat = pl.BlockSpec((TM, HGW), lambda i: (jnp.maximum(i - 1, 0), 0))
    return _pcall(body, name="readout_bwd", grid=(tt // TM,),
                  in_specs=[tok, tok, pl.BlockSpec((TM, HGW), lambda i: (i, C_GHG)), _full((1, HGW)), lat, ANY],
                  out_specs=[pl.BlockSpec((TM, HGW), lambda i: (i, C_GHG)), lat, _full((1, HGW))],
                  out_shape=[jax.ShapeDtypeStruct(dp.shape, BF16), jax.ShapeDtypeStruct((s_len, HGW), F32),
                             jax.ShapeDtypeStruct((1, HGW), F32)],
                  aliases={5: 0}, carry=carry)(o0, o1, p, hw4, dy, dp)


def _rope_tables(s_len):
    t = np.arange(s_len)
    inv = ROPE_THETA ** (-np.arange(0, 32, 2, dtype=np.float64) / 32)
    def half(pos):
        ang = pos[:, None].astype(np.float64) * inv[None, :]
        return (np.concatenate([np.cos(ang), np.cos(ang)], 1), np.concatenate([-np.sin(ang), np.sin(ang)], 1))
    cr, sr = half(t // GRID_W)
    cc, sc = half(t % GRID_W)
    cos = np.concatenate([cr, cc, cr, cc], 1)
    sin = np.concatenate([sr, sc, sr, sc], 1)
    cos = np.concatenate([np.ones((L, 128)), cos], 0)
    sin = np.concatenate([np.zeros((L, 128)), sin], 0)
    return jnp.asarray(cos, F32), jnp.asarray(sin, F32)


def _blockdiag(n, w):
    i = np.arange(n)
    return jnp.asarray((i[:, None] // w == i[None, :] // w) / float(w), F32)


def _dup_matrix():
    m = np.zeros((128, 512), np.float32)
    for g in range(2):
        for j in range(4):
            for dd in range(HDIM):
                m[64 * g + dd, 256 * g + 64 * j + dd] = 1.0
    return m


def _head_mean(x, blockdiag):
    return _dot(x, blockdiag, prec=lax.Precision.HIGH)


def _rot(x):
    n = x.shape[1]
    lane = lax.broadcasted_iota(jnp.int32, x.shape, 1)
    return jnp.where((lane % 32) < 16, pltpu.roll(x, n - 16, 1), pltpu.roll(x, 16, 1))


def _qk_prep(p, cos, sin, qnw8, knw2, bd512, bd128, dup):
    tt = p.shape[0]

    def body(q_ref, kv_ref, cos_ref, sin_ref, qw_ref, kw_ref, b5_ref, b1_ref, dup_ref,
             qr_ref, k4_ref, v4_ref):
        cos_, sin_ = cos_ref[...], sin_ref[...]
        q = q_ref[...]
        qn = q * lax.rsqrt(_head_mean(q * q, b5_ref[...]) + EPS) * qw_ref[...]
        cos4 = jnp.concatenate([cos_] * 4, axis=1)
        sin4 = jnp.concatenate([sin_] * 4, axis=1)
        qr_ref[...] = ((qn * cos4 + _rot(qn) * sin4) * (HDIM ** -0.5)).astype(BF16)
        kv = kv_ref[...]
        k, v = kv[:, :128], kv[:, 128:]
        kn = k * lax.rsqrt(_head_mean(k * k, b1_ref[...]) + EPS) * kw_ref[...]
        kr = kn * cos_ + _rot(kn) * sin_
        k4_ref[...] = _bdot(kr, dup_ref[...]).astype(BF16)
        v4_ref[...] = _bdot(v, dup_ref[...]).astype(BF16)

    row = lambda w, cb: pl.BlockSpec((TM, w), lambda i: (i, cb))
    out = jax.ShapeDtypeStruct((tt, ATW), BF16)
    return _pcall(body, name="qk_prep", grid=(tt // TM,),
                  in_specs=[row(ATW, C_QRAW), row(256, C_KV), row(128, 0), row(128, 0),
                            _full((1, ATW)), _full((1, 128)), _full((ATW, ATW)), _full((128, 128)),
                            _full((128, ATW))],
                  out_specs=[row(ATW, 0)] * 3, out_shape=[out] * 3)(
                      p, p, cos, sin, qnw8, knw2, bd512, bd128, dup)


def _attn_masks(i, nb):
    r = lax.broadcasted_iota(jnp.int32, (4 * BLK, 3 * BLK + L), 0) % BLK
    c = lax.broadcasted_iota(jnp.int32, (4 * BLK, 3 * BLK + L), 1)
    kpos = (i - 1) * BLK + c
    loc = (jnp.abs(c - BLK - r) <= BLK) & (kpos >= 0) & (kpos < nb * BLK)
    return loc | (c >= 3 * BLK)


def _stack_mask():
    r = lax.broadcasted_iota(jnp.int32, (4 * BLK, 256), 0)
    lane = lax.broadcasted_iota(jnp.int32, (4 * BLK, 256), 1)
    return (r // BLK) == (lane // HDIM)


def _stack_heads(xg, fill=0.0):
    x4 = jnp.concatenate([xg] * 4, axis=0)
    return jnp.where(_stack_mask(), x4, jnp.full_like(x4, fill))


def _unstack_heads(x4):
    out = jnp.where(_lane_mask(0), x4[0:BLK], 0.0)
    for j in range(1, 4):
        out = out + jnp.where(_lane_mask(j), x4[j * BLK:(j + 1) * BLK], 0.0)
    return out


def _per_head_rows(vals):
    return jnp.concatenate([jnp.broadcast_to(v, (BLK, 1)) for v in vals], axis=0)


def _lane_mask(j):
    lane = lax.broadcasted_iota(jnp.int32, (1, 256), 1)
    return (lane // HDIM) == j


def _attn_specs(nb):
    blk = lambda off: pl.BlockSpec((BLK, ATW), lambda i: (jnp.clip(i + off, 0, nb - 1) + 2, 0))
    ctx = pl.BlockSpec((L, ATW), lambda i: (0, 0))
    return blk, ctx


def _attn_fwd(qr, k4, v4, sinks, carry=None):
    tt = qr.shape[0]
    s_len = tt - L
    nb = s_len // BLK

    def body(sk_ref, q_ref, kp, ko, kn, kc, vp, vo, vn, vc, y_ref, lse_ref):
        i = pl.program_id(0)
        valid = _attn_masks(i, nb)
        q = q_ref[...]
        ys, lses = [], []
        for g in range(2):
            gs = slice(256 * g, 256 * g + 256)
            kcat = jnp.concatenate([kp[:, gs], ko[:, gs], kn[:, gs], kc[:, gs]], axis=0)
            vcat = jnp.concatenate([vp[:, gs], vo[:, gs], vn[:, gs], vc[:, gs]], axis=0)
            sink4 = _per_head_rows([sk_ref[4 * g + j] for j in range(4)])
            q4 = _stack_heads(q[:, gs])
            o_parts, l_parts = [], []
            for hp in range(2):
                rows = slice(2 * BLK * hp, 2 * BLK * (hp + 1))
                sink = sink4[rows]
                s = jnp.where(valid[rows], _dot(q4[rows], kcat, NT), -1e30)
                m = jnp.maximum(jnp.max(s, axis=-1, keepdims=True), sink)
                e = jnp.exp(s - m)
                den = jnp.sum(e, axis=-1, keepdims=True) + jnp.exp(sink - m)
                o_parts.append(_bdot(e * (1.0 / den), vcat))
                l_parts.append(jnp.broadcast_to(m + jnp.log(den), (2 * BLK, 256)))
            ys.append(_unstack_heads(jnp.concatenate(o_parts, axis=0)))
            lses.append(_unstack_heads(jnp.concatenate(l_parts, axis=0)))
        y_ref[...] = jnp.concatenate(ys, axis=1).astype(BF16)
        lse_ref[...] = jnp.concatenate(lses, axis=1)

    blk, ctx = _attn_specs(nb)
    out = pl.BlockSpec((BLK, ATW), lambda i: (i, 0))
    return _pcall(body, name="attn_fwd", grid=(nb,),
                  in_specs=[pl.BlockSpec(memory_space=pltpu.SMEM), blk(0),
                            blk(-1), blk(0), blk(1), ctx, blk(-1), blk(0), blk(1), ctx],
                  out_specs=[out, out],
                  out_shape=[jax.ShapeDtypeStruct((s_len, ATW), BF16),
                             jax.ShapeDtypeStruct((s_len, ATW), F32)], carry=carry)(
                      sinks, qr, k4, k4, k4, k4, v4, v4, v4, v4)


def _attn_bwd(qr, k4, v4, sinks, y, lse, dy, carry=None):
    tt = qr.shape[0]
    s_len = tt - L
    nb = s_len // BLK

    def body(sk_ref, q_ref, kp, ko, kn, kc, vp, vo, vn, vc, y_ref, lse_ref, dy_ref,
             dq_ref, dkw_ref, dvw_ref, dkc_ref, dvc_ref, dsk_ref):
        i = pl.program_id(0)

        @pl.when(i == 0)
        def _():
            dkc_ref[...] = jnp.zeros_like(dkc_ref)
            dvc_ref[...] = jnp.zeros_like(dvc_ref)
            dsk_ref[...] = jnp.zeros_like(dsk_ref)

        valid = _attn_masks(i, nb)
        q = q_ref[...]
        dy_ = dy_ref[...]
        dly = dy_ * y_ref[...].astype(F32)
        lse_ = lse_ref[...]
        dqs = []
        for g in range(2):
            gs = slice(256 * g, 256 * g + 256)
            kcat = jnp.concatenate([kp[:, gs], ko[:, gs], kn[:, gs], kc[:, gs]], axis=0)
            vcat = jnp.concatenate([vp[:, gs], vo[:, gs], vn[:, gs], vc[:, gs]], axis=0)
            q4 = _stack_heads(q[:, gs])
            dy4 = _stack_heads(dy_[:, gs]).astype(BF16)
            lse4 = jnp.max(_stack_heads(lse_[:, gs], fill=-1e30), axis=-1, keepdims=True)
            delta = jnp.sum(_stack_heads(dly[:, gs]), axis=-1, keepdims=True)
            sink = _per_head_rows([sk_ref[4 * g + j] for j in range(4)])
            pr = jnp.where(valid, jnp.exp(_dot(q4, kcat, NT) - lse4), 0.0)
            dsb = (pr * (_dot(dy4, vcat, NT) - delta)).astype(BF16)
            dsink = jnp.exp(sink - lse4) * delta
            for j in range(4):
                dsk_ref[4 * g + j:4 * g + j + 1, :] += jnp.broadcast_to(
                    -jnp.sum(dsink[j * BLK:(j + 1) * BLK], axis=0, keepdims=True), (1, 128))
            dqs.append(_unstack_heads(_dot(dsb, kcat)))
            dkg = _dot(dsb, q4, TN)
            dvg = _dot(pr.astype(BF16), dy4, TN)
            dkw_ref[0, :, gs] = dkg[:3 * BLK]
            dvw_ref[0, :, gs] = dvg[:3 * BLK]
            dkc_ref[:, gs] += dkg[3 * BLK:]
            dvc_ref[:, gs] += dvg[3 * BLK:]
        dq_ref[...] = jnp.concatenate(dqs, axis=1)

    blk, ctx = _attn_specs(nb)
    out = pl.BlockSpec((BLK, ATW), lambda i: (i, 0))
    win = pl.BlockSpec((1, 3 * BLK, ATW), lambda i: (i, 0, 0))
    acc = _full((L, ATW))
    return _pcall(body, name="attn_bwd", grid=(nb,),
                  in_specs=[pl.BlockSpec(memory_space=pltpu.SMEM), blk(0),
                            blk(-1), blk(0), blk(1), ctx, blk(-1), blk(0), blk(1), ctx, out, out, out],
                  out_specs=[out, win, win, acc, acc, _full((8, 128))],
                  out_shape=[jax.ShapeDtypeStruct((s_len, ATW), F32),
                             jax.ShapeDtypeStruct((nb, 3 * BLK, ATW), F32),
                             jax.ShapeDtypeStruct((nb, 3 * BLK, ATW), F32),
                             jax.ShapeDtypeStruct((L, ATW), F32), jax.ShapeDtypeStruct((L, ATW), F32),
                             jax.ShapeDtypeStruct((8, 128), F32)], carry=carry)(
                      sinks, qr, k4, k4, k4, k4, v4, v4, v4, v4, y, lse, dy)


def _attn_post(p, cos, sin, qnw8, knw2, bd512, bd128, dupt, dq, dkw, dvw, dkc, dvc, dp, carry=None):
    tt = p.shape[0]
    s_len = tt - L
    nb = s_len // BLK
    nctx = L // BLK

    def body(q_ref, kv_ref, cos_ref, sin_ref, qw_ref, kw_ref, b5_ref, b1_ref, dupt_ref,
             dq_ref, kwp, kwo, kwn, vwp, vwo, vwn, dkc_ref, dvc_ref, _dp_in,
             dp_ref, dqw_ref, dkw_ref):
        t = pl.program_id(0)
        j = t - nctx

        @pl.when(t == 0)
        def _():
            dqw_ref[...] = jnp.zeros_like(dqw_ref)
            dkw_ref[...] = jnp.zeros_like(dkw_ref)

        is_lat = t >= nctx
        cos_, sin_ = cos_ref[...], sin_ref[...]
        has_p = is_lat & (j >= 1)
        has_n = is_lat & (j <= nb - 2)
        dk4 = (jnp.where(is_lat, kwo[0], dkc_ref[...]) + jnp.where(has_p, kwp[0], 0.0)
               + jnp.where(has_n, kwn[0], 0.0))
        dv4 = (jnp.where(is_lat, vwo[0], dvc_ref[...]) + jnp.where(has_p, vwp[0], 0.0)
               + jnp.where(has_n, vwn[0], 0.0))
        dkr = _dot(dk4, dupt_ref[...], prec=HI)
        dv = _dot(dv4, dupt_ref[...], prec=HI)
        kv = kv_ref[...]
        k = kv[:, :128]
        kw = kw_ref[...]
        rk = lax.rsqrt(_head_mean(k * k, b1_ref[...]) + EPS)
        xk = k * rk
        dkn = dkr * cos_ + _rot(dkr * sin_)
        dxk = dkn * kw
        dk = rk * (dxk - xk * _head_mean(dxk * xk, b1_ref[...]))
        dkw_ref[...] += jnp.sum(dkn * xk, axis=0, keepdims=True)
        q = q_ref[...]
        qw = qw_ref[...]
        rq = lax.rsqrt(_head_mean(q * q, b5_ref[...]) + EPS)
        xq = q * rq
        cos4 = jnp.concatenate([cos_] * 4, axis=1)
        sin4 = jnp.concatenate([sin_] * 4, axis=1)
        dqr = jnp.where(is_lat, dq_ref[...], 0.0) * (HDIM ** -0.5)
        dqn = dqr * cos4 + _rot(dqr * sin4)
        dxq = dqn * qw
        dqraw = rq * (dxq - xq * _head_mean(dxq * xq, b5_ref[...]))
        dqw_ref[...] += jnp.sum(dqn * xq, axis=0, keepdims=True)
        dp_ref[...] = jnp.concatenate([dqraw, dk, dv], axis=1).astype(BF16)

    row = lambda w, cb: pl.BlockSpec((BLK, w), lambda t: (t, cb))
    lat = pl.BlockSpec((BLK, ATW), lambda t: (jnp.maximum(t - nctx, 0), 0))

    def part(off):
        return pl.BlockSpec((1, BLK, ATW), lambda t: (jnp.clip(t - nctx + off, 0, nb - 1), 1 - off, 0))

    cacc = pl.BlockSpec((BLK, ATW), lambda t: (jnp.minimum(t, nctx - 1), 0))
    return _pcall(body, name="attn_post", grid=(tt // BLK,),
                  in_specs=[row(ATW, C_QRAW), row(256, C_KV), row(128, 0), row(128, 0),
                            _full((1, ATW)), _full((1, 128)), _full((ATW, ATW)), _full((128, 128)),
                            _full((ATW, 128)), lat, part(-1), part(0), part(1), part(-1), part(0), part(1),
                            cacc, cacc, ANY],
                  out_specs=[pl.BlockSpec((BLK, 768), lambda t: (t, C_QKV)), _full((1, ATW)), _full((1, 128))],
                  out_shape=[jax.ShapeDtypeStruct(dp.shape, BF16), jax.ShapeDtypeStruct((1, ATW), F32),
                             jax.ShapeDtypeStruct((1, 128), F32)],
                  aliases={18: 0}, carry=carry)(p, p, cos, sin, qnw8, knw2, bd512, bd128, dupt,
                                   dq, dkw, dkw, dkw, dvw, dvw, dvw, dkc, dvc, dp)


def _branch_merge(y_hg, y_at, bh4, ba4, p):
    s_len = y_hg.shape[0]

    def body(yh_ref, ya_ref, bh_ref, ba_ref, gh_ref, ga_ref, ah_ref, aa_ref, m_ref):
        yh, ya = yh_ref[...], ya_ref[...]
        ah = jnp.concatenate([_bdot(yh, bh_ref[j]) for j in range(4)], axis=1)
        aa = jnp.concatenate([_bdot(ya, ba_ref[j]) for j in range(4)], axis=1)
        ah_ref[...] = ah.astype(BF16)
        aa_ref[...] = aa.astype(BF16)
        m_ref[...] = (_sig(gh_ref[...]) * ah + _sig(ga_ref[...]) * aa).astype(BF16)

    row = pl.BlockSpec((TM, D), lambda i: (i, 0))
    y = pl.BlockSpec((TM, HGW), lambda i: (i, 0))
    f = jax.ShapeDtypeStruct((s_len, D), BF16)
    return _pcall(body, name="branch_merge", grid=(s_len // TM,),
                  in_specs=[y, y, _full(bh4.shape), _full(ba4.shape),
                            pl.BlockSpec((TM, D), lambda i: (i + 1, 2)), pl.BlockSpec((TM, D), lambda i: (i + 1, 3))],
                  out_specs=[row, row, row],
                  out_shape=[f, f, jax.ShapeDtypeStruct((s_len, D), BF16)])(y_hg, y_at, bh4, ba4, p, p)


def _branch_bwd(dmh, dma, bh4, ba4, y_hg, y_at):
    s_len = dmh.shape[0]
    nk = s_len // TS
    ns = D // 4

    def body(dh_ref, da_ref, bh_ref, ba_ref, yh_ref, ya_ref, dyh_ref, dya_ref, gh_ref, ga_ref, acc_h, acc_a):
        t = pl.program_id(0)

        @pl.when(t == 0)
        def _():
            acc_h[...] = jnp.zeros_like(acc_h)
            acc_a[...] = jnp.zeros_like(acc_a)

        for d_ref, w_ref, y_ref, dy_ref, acc in ((dh_ref, bh_ref, yh_ref, dyh_ref, acc_h),
                                                 (da_ref, ba_ref, ya_ref, dya_ref, acc_a)):
            y = y_ref[...]
            dy = jnp.zeros((TS, HGW), F32)
            for j in range(4):
                dj = d_ref[:, j * ns:(j + 1) * ns]
                dy = dy + _bdot(dj, w_ref[j], NT)
                acc[j] += _bdot(y, dj, TN)
            dy_ref[...] = dy

        @pl.when(t == nk - 1)
        def _():
            gh_ref[...] = acc_h[...].astype(BF16)
            ga_ref[...] = acc_a[...].astype(BF16)

    dm = pl.BlockSpec((TS, D), lambda t: (t, 0))
    y = pl.BlockSpec((TS, HGW), lambda t: (t, 0))
    w = _full(bh4.shape)
    fy = jax.ShapeDtypeStruct((s_len, HGW), F32)
    gw = jax.ShapeDtypeStruct(bh4.shape, BF16)
    return _pcall(body, name="branch_bwd", grid=(nk,), in_specs=[dm, dm, w, w, y, y],
                  out_specs=[y, y, w, w], out_shape=[fy, fy, gw, gw],
                  scratch=[pltpu.VMEM(bh4.shape, F32)] * 2)(dmh, dma, bh4, ba4, y_hg, y_at)


def _merge_bwd(dattn, w_o, mixed, ah, aa, p, carry=None):
    tt = p.shape[0]
    s_len = tt - L
    nt = tt // TM

    def body(da_ref, wo_ref, mx_ref, ah_ref, aa_ref, gh_ref, ga_ref, dp_ref, dmh_ref, dma_ref, go_ref, acc):
        i = pl.program_id(0)

        @pl.when(i == 0)
        def _():
            dp_ref[...] = jnp.zeros_like(dp_ref)
            acc[...] = jnp.zeros_like(acc)

        @pl.when(i >= 1)
        def _():
            da = da_ref[...]
            acc[...] += _bdot(mx_ref[...], da, TN)
            dm_ = _bdot(da, wo_ref[...], NT)
            sh, sa = _sig(gh_ref[...]), _sig(ga_ref[...])
            dp_ref[...] = jnp.concatenate([dm_ * ah_ref[...].astype(F32) * sh * (1.0 - sh),
                                           dm_ * aa_ref[...].astype(F32) * sa * (1.0 - sa)], axis=1).astype(BF16)
            dmh_ref[...] = (dm_ * sh).astype(BF16)
            dma_ref[...] = (dm_ * sa).astype(BF16)

        @pl.when(i == nt - 1)
        def _():
            go_ref[...] = acc[...].astype(BF16)

    lat = pl.BlockSpec((TM, D), lambda i: (jnp.maximum(i - 1, 0), 0))
    return _pcall(body, name="merge_bwd", grid=(nt,),
                  in_specs=[lat, _full((D, D)), lat, lat, lat, pl.BlockSpec((TM, D), lambda i: (i, 2)),
                            pl.BlockSpec((TM, D), lambda i: (i, 3))],
                  out_specs=[pl.BlockSpec((TM, 2 * D), lambda i: (i, C_GATES)), lat, lat, _full((D, D))],
                  out_shape=[jax.ShapeDtypeStruct((tt, NCOL), BF16), jax.ShapeDtypeStruct((s_len, D), BF16),
                             jax.ShapeDtypeStruct((s_len, D), BF16), jax.ShapeDtypeStruct((D, D), BF16)],
                  scratch=[pltpu.VMEM((D, D), F32)], carry=carry)(dattn, w_o, mixed, ah, aa, p, p)


def _local_step(x, ctx, tgt, mod, modc, nw1, nw2, lg, hw, qnw, knw, sinks,
                w_in, wts, dist=None):
    s_len = x.shape[0]
    tt = s_len + L
    ss1 = jnp.stack([modc, mod[0:2]])
    ss2 = mod[3:5][None]
    g1, g2 = mod[2:3], mod[5:6]
    hw4 = jnp.tile(hw, (1, 4))
    qnw8 = jnp.tile(qnw, (1, 8))
    knw2 = jnp.tile(knw, (1, 2))
    cos, sin = _rope_tables(s_len)
    bd512, bd128 = _blockdiag(ATW, HDIM), _blockdiag(128, HDIM)
    dupm = _dup_matrix()
    dup, dupt = jnp.asarray(dupm, BF16), jnp.asarray(dupm.T, F32)
    tmt = tt

    def four(b):
        return b.reshape(4, 2 * b.shape[1], b.shape[2])

    def halves(g):
        return g.reshape(4, 2, g.shape[1] // 2, g.shape[2])

    h = _mod1(ctx, x, nw1, ss1)
    if dist is None:
        bh4, ba4, w_o, g4, u4, dn4 = wts
        p = _mm_in(h, w_in, tmt)
        o0, st0 = _hgrn_fwd(p, lg, rev=False)
        o1, st1, y_hg = _hgrn_fwd(p, lg, rev=True, readout=(o0, hw4))
    else:
        core, chip = dist
        half = wts[3].shape[1] // 2
        p, (o8, g8a) = _mm_in(h, w_in, tmt, carry=_carry_gather([wts[2], wts[3]], rows=[None, (0, half)]))
        (o0, st0), (g8,) = _hgrn_fwd(p, lg, rev=False, carry=_carry_gather([g8a], rows=[(half, half)]))
        (o1, st1, y_hg), (dn8a, bh8, ba8) = _hgrn_fwd(
            p, lg, rev=True, readout=(o0, hw4),
            carry=_carry_gather([wts[5], wts[0], wts[1]], rows=[(0, half), None, None]))
        bh4, ba4, w_o, g4 = four(bh8), four(ba8), four(o8).reshape(D, D), four(g8)
    qr, k4, v4 = _qk_prep(p, cos, sin, qnw8, knw2, bd512, bd128, dup)
    if dist is None:
        y_at, lse = _attn_fwd(qr, k4, v4, sinks)
    else:
        (y_at, lse), (u8, dn8) = _attn_fwd(qr, k4, v4, sinks,
                                           carry=_carry_gather([wts[4], dn8a], rows=[None, (half, half)]))
        u4, dn4 = four(u8), four(dn8)
    ah, aa, mixed = _branch_merge(y_hg, y_at, bh4, ba4, p)
    ao, x1, h2 = _out_proj_mod2(mixed, w_o, x, g1, nw2, ss2)
    a4, b4, z4 = _ffn_up(h2, g4, u4)
    sq, dx2, dyb, dg2 = _ffn_down_loss(z4, dn4, x1, g2, tgt)

    da4, db4 = _ffn_dz(dyb, dn4, a4, b4)
    g_dn = _ffn_gdn(z4, dyb)
    if dist is None:
        dh2 = _ffn_dh2(da4, db4, g4, u4)
    else:
        dn_units = [halves(g_dn)]
        dh2, dn_recv = _ffn_dh2(da4, db4, g4, u4, carry=_carry_pairx(dn_units))
        dn_pairs = _rs_pair_add(dn_units, dn_recv, core)
    if dist is None:
        g_g, g_u = _ffn_ggu(h2, da4, db4)
    else:
        (g_g, g_u), c_dn = _ffn_ggu(h2, da4, db4, carry=_carry_chipx(dn_pairs))
        red_dn = _rs_chip_add(dn_pairs, c_dn, core, chip)
    dx1, dattn, dss2, dnw2, dg1 = _mod2_bwd(x1, dh2, dx2, ao, nw2, ss2, g1)
    if dist is None:
        dp, dmh, dma, g_o = _merge_bwd(dattn, w_o, mixed, ah, aa, p)
    else:
        gu_units = [halves(g_g), halves(g_u)]
        (dp, dmh, dma, g_o), gu_recv = _merge_bwd(dattn, w_o, mixed, ah, aa, p, carry=_carry_pairx(gu_units))
        ffn_pairs = list(dn_pairs) + list(_rs_pair_add(gu_units, gu_recv, core))
    dy_hg, dy_at, g_bh, g_ba = _branch_bwd(dmh, dma, bh4, ba4, y_hg, y_at)
    if dist is None:
        dp, do, dhw4 = _readout_bwd(o0, o1, p, hw4, dy_hg, dp)
        dq, dkw, dvw, dkc, dvc, dsk = _attn_bwd(qr, k4, v4, sinks, y_at, lse, dy_at)
        dp, dqnw8, dknw2 = _attn_post(p, cos, sin, qnw8, knw2, bd512, bd128, dupt, dq, dkw, dvw, dkc, dvc, dp)
    else:
        mix_units = [halves(g_bh), halves(g_ba), halves(g_o.reshape(4, D // 4, D))]
        (dp, do, dhw4), mix_recv = _readout_bwd(o0, o1, p, hw4, dy_hg, dp, carry=_carry_pairx(mix_units))
        mix_pairs = _rs_pair_add(mix_units, mix_recv, core)
        (dq, dkw, dvw, dkc, dvc, dsk), bwd = _attn_bwd(
            qr, k4, v4, sinks, y_at, lse, dy_at,
            carry=_carry_join(_carry_chipx(ffn_pairs[1:2]), _carry_sibx(red_dn)))
        red_g = _rs_chip_add(ffn_pairs[1:2], bwd[0:1], core, chip)
        (dp, dqnw8, dknw2), post = _attn_post(
            p, cos, sin, qnw8, knw2, bd512, bd128, dupt, dq, dkw, dvw, dkc, dvc, dp, carry=_carry_sibx(red_g))
    if dist is None:
        dp, dv0, dq0, dlg0 = _hgrn_bwd(p, lg, do, st0, dp, None, rev=False)
        dp, dlg1 = _hgrn_bwd(p, lg, do, st1, dp, (dv0, dq0), rev=True)
    else:
        (dp, dv0, dq0, dlg0), c_u = _hgrn_bwd(p, lg, do, st0, dp, None, rev=False,
                                              carry=_carry_chipx(ffn_pairs[2:3]))
        red_u = _rs_chip_add(ffn_pairs[2:3], c_u, core, chip)
        (dp, dlg1), last = _hgrn_bwd(p, lg, do, st1, dp, (dv0, dq0), rev=True,
                                     carry=_carry_join(_carry_chipx(mix_pairs), _carry_sibx(red_u)))
        mix_reds = _rs_chip_add(mix_pairs, last[0:3], core, chip)
        ffn_done = bwd[1:2] + post[0:1] + last[3:4]
    g_in = _mm_gin(dp, h, tmt)
    if dist is None:
        dh = _mm_dh(dp, w_in, tmt)
        gx, dss1, dnw1 = _mod1_bwd(ctx, x, dh, dx1, nw1, ss1)
        rs = None
    else:
        in_units = [halves(g_in.reshape(4, NCOL // 4, D))]
        in_pairs = _rs_pair_add(in_units, _pair_exchange(in_units), core)
        first_rows, rest_rows = _split_rows(in_pairs[0].shape[1], IN_ROWS_WITH_DH)
        dh, both = _mm_dh(dp, w_in, tmt, carry=_carry_join(_carry_chipx(in_pairs, rows=first_rows),
                                                           _carry_sibx(mix_reds)))
        in_part, mix_done = both[0:1], both[1:4]
        gx, dss1, dnw1 = _mod1_bwd(ctx, x, dh, dx1, nw1, ss1)
        rs = dict(ffn_done=ffn_done, mix_done=mix_done, in_pairs=in_pairs, in_part=in_part, rest_rows=rest_rows)

    dmod = jnp.concatenate([dss1[1], dg1, dss2, dg2], axis=0)
    dmodc = dss1[0]
    raw = (dss1, dg1, dss2, dg2, dnw1, dnw2, dhw4, dqnw8, dknw2, dsk, dlg0, dlg1)
    small = dict(raw=raw, dmod=dmod, dmodc=dmodc, dnw1=dnw1, dnw2=dnw2,
                 dhw=dhw4.reshape(4, HGD).sum(0, keepdims=True),
                 dqnw=dqnw8.reshape(8, HDIM).sum(0, keepdims=True),
                 dknw=dknw2.reshape(2, HDIM).sum(0, keepdims=True),
                 dsinks=dsk[:, 0], dlg=jnp.concatenate([dlg0, dlg1], axis=0))
    big = dict(w_in=g_in, w_bh=g_bh, w_ba=g_ba, w_o=g_o, w_g=g_g, w_u=g_u, w_dn=g_dn)
    return sq, gx, big, small, rs


def _place():
    x, y, c = lax.axis_index("x"), lax.axis_index("y"), lax.axis_index("c")
    return x, y, c


def _gather_blocks(x_refs, out_refs, send_sems, recv_sems, local_sems):
    n = len(out_refs)
    x, y, c = _place()
    me, sibling = (x, y, c), (x, y, 1 - c)
    chips = [(1 - x, y), (x, 1 - y), (1 - x, 1 - y)]

    def slot(u, px, py, pc):
        return out_refs[u].at[4 * px + 2 * py + pc]

    def copy(u, k, block, to, src=None):
        return pltpu.make_async_remote_copy(
            src_ref=slot(u, *block) if src is None else src, dst_ref=slot(u, *block),
            send_sem=send_sems.at[u, k], recv_sem=recv_sems.at[u, k], device_id=to, device_id_type=MESH)

    mines = [pltpu.make_async_copy(x_refs[u], slot(u, *me), local_sems.at[u]) for u in range(n)]
    for cp in mines:
        cp.start()
    first = []
    for u in range(n):
        first.append(copy(u, 0, me, sibling, src=x_refs[u]))
        first += [copy(u, 1 + j, me, (*chip, c), src=x_refs[u]) for j, chip in enumerate(chips)]
    for cp in first:
        cp.start()
    passed = []
    for j, chip in enumerate(chips):
        for u in range(n):
            copy(u, 1 + j, (*chip, c), me).wait_recv()
            fwd = copy(u, 4 + j, (*chip, c), sibling)
            fwd.start()
            passed.append(fwd)
    for u in range(n):
        copy(u, 0, sibling, me).wait_recv()
    for j, chip in enumerate(chips):
        for u in range(n):
            copy(u, 4 + j, (*chip, 1 - c), me).wait_recv()
    for cp in first + passed:
        cp.wait_send()
    for cp in mines:
        cp.wait()


def _gather_sems(n):
    return [pltpu.SemaphoreType.DMA((n, 7)), pltpu.SemaphoreType.DMA((n, 7)), pltpu.SemaphoreType.DMA((n,))]


def _cast_place(ws, c, dev):
    n = len(ws)

    def body(s_ref, *refs):
        for u in range(n):
            refs[n + u][0] = refs[u][...].astype(BF16)

    in_specs, out_specs, out_shape = [], [], []
    for w in ws:
        q, cols = w.shape[0] // 4, w.shape[1]
        in_specs.append(pl.BlockSpec((q, cols), lambda i, s: (2 * s[0] + i, 0)))
        out_specs.append(pl.BlockSpec((1, q, cols), lambda i, s: (s[1], i, 0)))
        out_shape.append(jax.ShapeDtypeStruct((8, 2 * q, cols), BF16))
    return pl.pallas_call(
        body, name="cast_place",
        grid_spec=pltpu.PrefetchScalarGridSpec(num_scalar_prefetch=1, grid=(2,), in_specs=in_specs,
                                               out_specs=out_specs),
        out_shape=_out_hbm(out_shape),
        compiler_params=pltpu.CompilerParams(vmem_limit_bytes=48 << 20))(jnp.stack([c, dev]), *_in_hbm(ws))


def _gather_phases(out_refs, send_sems, recv_sems, rows=None):
    n = len(out_refs)
    x, y, c = _place()
    me, sibling = (x, y, c), (x, y, 1 - c)
    chips = [(1 - x, y), (x, 1 - y), (1 - x, 1 - y)]

    def copy(u, k, block, to):
        px, py, pc = block
        ref = out_refs[u].at[4 * px + 2 * py + pc]
        if rows is not None and rows[u] is not None:
            ref = ref.at[pl.ds(rows[u][0], rows[u][1])]
        return pltpu.make_async_remote_copy(src_ref=ref, dst_ref=ref, send_sem=send_sems.at[u, k],
                                            recv_sem=recv_sems.at[u, k], device_id=to, device_id_type=MESH)

    def start():
        for u in range(n):
            copy(u, 0, me, sibling).start()
            for j, chip in enumerate(chips):
                copy(u, 1 + j, me, (*chip, c)).start()

    def mid():
        for j, chip in enumerate(chips):
            for u in range(n):
                copy(u, 1 + j, (*chip, c), me).wait_recv()
                copy(u, 4 + j, (*chip, c), sibling).start()

    def end():
        for u in range(n):
            copy(u, 0, sibling, me).wait_recv()
        for j, chip in enumerate(chips):
            for u in range(n):
                copy(u, 4 + j, (*chip, 1 - c), me).wait_recv()
        for u in range(n):
            copy(u, 0, me, sibling).wait_send()
            for j, chip in enumerate(chips):
                copy(u, 1 + j, me, (*chip, c)).wait_send()
                copy(u, 4 + j, (*chip, c), sibling).wait_send()

    return start, mid, end


def _carry_gather(bufs, rows=None):
    n = len(bufs)
    return _Carry(bufs, [jax.ShapeDtypeStruct(b.shape, b.dtype) for b in bufs], {u: u for u in range(n)},
                  [pltpu.SemaphoreType.DMA((n, 7)), pltpu.SemaphoreType.DMA((n, 7))],
                  lambda ins, outs, sems: _gather_phases(outs, *sems, rows=rows), "both")


def _ag_small(raw, sq):
    def body(dss1, dg1, dss2, dg2, dnw1, dnw2, dhw4, dqnw8, dknw2, dsk, dlg0, dlg1, sq_ref,
             out_ref, tot_ref, blk, send_sems, recv_sems, local_sems):
        _peer_barrier("both")
        blk[...] = jnp.zeros_like(blk)
        blk[0:2, :] = dss1[1]
        blk[2:3, :] = dg1[...]
        blk[3:5, :] = dss2[...]
        blk[5:6, :] = dg2[...]
        blk[6:8, :] = dss1[0]
        blk[8:9, :] = dnw1[...]
        blk[9:10, :] = dnw2[...]
        blk[10:11, 0:HGW] = dhw4[...]
        blk[10:11, HGW:D] = dqnw8[...]
        blk[11:12, 0:128] = dknw2[...]
        blk[12:14, 0:HGW] = dlg0[0]
        blk[14:16, 0:HGW] = dlg1[0]
        blk[16:24, 0:128] = dsk[...]
        blk[24:25, :] = sq_ref[...]
        _gather_blocks([blk], [out_ref], send_sems, recv_sems, local_sems)
        acc = out_ref[0]
        for i in range(1, 8):
            acc = acc + out_ref[i]
        tot_ref[...] = acc

    vm = pl.BlockSpec(memory_space=pltpu.VMEM)
    return pl.pallas_call(
        body, name="ag_small",
        out_shape=[jax.ShapeDtypeStruct((8, 32, D), F32), jax.ShapeDtypeStruct((32, D), F32)],
        in_specs=[vm] * 13, out_specs=[vm, vm],
        scratch_shapes=[pltpu.VMEM((32, D), F32)] + _gather_sems(1),
        compiler_params=pltpu.CompilerParams(collective_id=BARRIER_IDS["both"]))(*raw, sq)


def _pairx_shapes(units):
    return [jax.ShapeDtypeStruct((4,) + g.shape[2:], g.dtype) for g in units]


def _pairx_phases(g_refs, r_refs, send_sems, recv_sems):
    n = len(g_refs)
    x, y, c = _place()
    cps = [pltpu.make_async_remote_copy(
        src_ref=g_refs[u].at[j, 1 - c], dst_ref=r_refs[u].at[j], send_sem=send_sems.at[u, j],
        recv_sem=recv_sems.at[u, j], device_id=(x, y, 1 - c), device_id_type=MESH)
        for u in range(n) for j in range(4)]

    def start():
        for cp in cps:
            cp.start()

    def end():
        for cp in cps:
            cp.wait()

    return start, None, end


def _carry_pairx(units):
    n = len(units)
    return _Carry(units, _pairx_shapes(units), {},
                  [pltpu.SemaphoreType.DMA((n, 4)), pltpu.SemaphoreType.DMA((n, 4))],
                  lambda ins, outs, sems: _pairx_phases(ins, outs, *sems), "sib")


def _pair_exchange(units):
    n = len(units)

    def body(*refs):
        _peer_barrier("sib")
        start, _, end = _pairx_phases(refs[:n], refs[n:2 * n], *refs[2 * n:])
        start()
        end()

    return pl.pallas_call(
        body, name="pair_exchange", out_shape=_pairx_shapes(units), in_specs=[ANY] * n, out_specs=[ANY] * n,
        scratch_shapes=[pltpu.SemaphoreType.DMA((n, 4))] * 2,
        compiler_params=pltpu.CompilerParams(collective_id=BARRIER_IDS["sib"]))(*_in_hbm(units))


IN_ROWS_WITH_DH = 0.6


def _split_rows(h, share):
    first = int(h * share) // BF16_SUBLANES * BF16_SUBLANES
    return (0, first), (first, h - first)


def _rs_pair_add(units, recvs, c):
    n = len(units)

    def body(c_ref, *refs):
        for u in range(n):
            refs[2 * n + u][...] = (refs[u][0].astype(F32) + refs[n + u][...].astype(F32)).astype(BF16)

    in_specs, out_specs, out_shape = [], [], []
    for g in units:
        h, w = g.shape[2:]
        in_specs.append(pl.BlockSpec((1, 1, h, w), lambda j, cr: (j, cr[0], 0, 0)))
    for g in units:
        h, w = g.shape[2:]
        in_specs.append(pl.BlockSpec((1, h, w), lambda j, cr: (j, 0, 0)))
        out_specs.append(pl.BlockSpec((1, h, w), lambda j, cr: (j, 0, 0)))
        out_shape.append(jax.ShapeDtypeStruct((4, h, w), BF16))
    return pl.pallas_call(
        body, name="rs_pair_add",
        grid_spec=pltpu.PrefetchScalarGridSpec(num_scalar_prefetch=1, grid=(4,), in_specs=in_specs,
                                               out_specs=out_specs),
        out_shape=_out_hbm(out_shape),
        compiler_params=pltpu.CompilerParams(vmem_limit_bytes=48 << 20))(
            c.reshape(1), *_in_hbm(list(units) + list(recvs)))


def _chipx_phases(p_refs, r_refs, send_sems, recv_sems, rows=None):
    n = len(p_refs)
    x, y, c = _place()
    k = 2 * x + y

    def part(ref):
        return ref if rows is None else ref.at[pl.ds(rows[0], rows[1])]

    sends = []
    for d in range(1, 4):
        j = (k + d) % 4
        for u in range(n):
            sends.append(pltpu.make_async_remote_copy(
                src_ref=part(p_refs[u].at[j]), dst_ref=part(r_refs[u].at[k]), send_sem=send_sems.at[u, d - 1],
                recv_sem=recv_sems.at[u, d - 1], device_id=(j // 2, j % 2, c), device_id_type=MESH))

    def start():
        for cp in sends:
            cp.start()

    def end():
        for d in range(1, 4):
            src = (k + 4 - d) % 4
            for u in range(n):
                pltpu.make_async_remote_copy(
                    src_ref=part(p_refs[u].at[src]), dst_ref=part(r_refs[u].at[src]),
                    send_sem=send_sems.at[u, d - 1], recv_sem=recv_sems.at[u, d - 1], device_id=(x, y, c),
                    device_id_type=MESH).wait_recv()
        for cp in sends:
            cp.wait_send()

    return start, None, end


def _carry_chipx(pairs, rows=None, into=None):
    n = len(pairs)
    sems = [pltpu.SemaphoreType.DMA((n, 3)), pltpu.SemaphoreType.DMA((n, 3))]
    shapes = [jax.ShapeDtypeStruct(p.shape, p.dtype) for p in pairs]
    if into is None:
        return _Carry(pairs, shapes, {}, sems, lambda ins, outs, s: _chipx_phases(ins, outs, *s, rows=rows),
                      "chips")
    return _Carry(list(pairs) + list(into), shapes, {n + u: u for u in range(n)}, sems,
                  lambda ins, outs, s: _chipx_phases(ins[:n], outs, *s, rows=rows), "chips")


def _rs_chip_add(pairs, contribs, c, chip):
    n = len(pairs)

    def body(s_ref, *refs):
        for u in range(n):
            a, b, c_, d = refs[4 * u:4 * u + 4]
            refs[4 * n + u][0] = ((a[0].astype(F32) + b[0].astype(F32)) + c_[0].astype(F32)) + d[0].astype(F32)

    in_specs, out_specs, out_shape, args = [], [], [], []
    for p, r in zip(pairs, contribs):
        h, w = p.shape[1] // 2, p.shape[2]
        in_specs += [pl.BlockSpec((1, h, w), functools.partial(lambda d, i, s: ((s[1] + d) % 4, i, 0), d))
                     for d in range(4)]
        args += [p, r, r, r]
        out_specs.append(pl.BlockSpec((1, h, w), lambda i, s: (s[0], i, 0)))
        out_shape.append(jax.ShapeDtypeStruct((2, 2 * h, w), F32))
    return pl.pallas_call(
        body, name="rs_chip_add",
        grid_spec=pltpu.PrefetchScalarGridSpec(num_scalar_prefetch=1, grid=(2,), in_specs=in_specs,
                                               out_specs=out_specs),
        out_shape=_out_hbm(out_shape),
        compiler_params=pltpu.CompilerParams(vmem_limit_bytes=48 << 20))(jnp.stack([c, chip]), *_in_hbm(args))


def _rs_sibling_gather(reds, blks):
    n, k = len(reds), len(blks)
    vm = pl.BlockSpec(memory_space=pltpu.VMEM)

    def body(*refs):
        blk_in, red_out, blk_out = refs[n:n + k], refs[n + k:2 * n + k], refs[2 * n + k:2 * (n + k)]
        sems = refs[2 * (n + k):]
        _peer_barrier("both")
        start, _, end = _sibx_phases(red_out, *sems[:2])
        start()
        _gather_blocks(blk_in, blk_out, *sems[2:])
        end()

    res = pl.pallas_call(
        body, name="rs_sibling_gather",
        out_shape=[jax.ShapeDtypeStruct(r.shape, r.dtype) for r in reds]
        + [jax.ShapeDtypeStruct((8,) + b.shape, b.dtype) for b in blks],
        in_specs=[ANY] * n + [vm] * k, out_specs=[ANY] * n + [vm] * k,
        input_output_aliases={u: u for u in range(n)},
        scratch_shapes=[pltpu.SemaphoreType.DMA((n,))] * 2 + _gather_sems(k),
        compiler_params=pltpu.CompilerParams(collective_id=BARRIER_IDS["both"]))(*_in_hbm(reds), *blks)
    return res[:n], res[n:]


def _sibx_phases(o_refs, send_sems, recv_sems):
    n = len(o_refs)
    x, y, c = _place()
    cps = [pltpu.make_async_remote_copy(
        src_ref=o_refs[u].at[c], dst_ref=o_refs[u].at[c], send_sem=send_sems.at[u], recv_sem=recv_sems.at[u],
        device_id=(x, y, 1 - c), device_id_type=MESH) for u in range(n)]

    def start():
        for cp in cps:
            cp.start()

    def end():
        for u in range(n):
            cps[u].wait_send()
            pltpu.make_async_remote_copy(
                src_ref=o_refs[u].at[1 - c], dst_ref=o_refs[u].at[1 - c], send_sem=send_sems.at[u],
                recv_sem=recv_sems.at[u], device_id=(x, y, 1 - c), device_id_type=MESH).wait_recv()

    return start, None, end


def _carry_sibx(reds):
    n = len(reds)
    return _Carry(reds, [jax.ShapeDtypeStruct(r.shape, r.dtype) for r in reds], {u: u for u in range(n)},
                  [pltpu.SemaphoreType.DMA((n,))] * 2, lambda ins, outs, sems: _sibx_phases(outs, *sems), "sib")


def _prologue(blk, c_ctx, w, b, in8):
    n = w.shape[1]

    def body(blk_ref, cctx_ref, w_ref, b_ref, _in_in, g0_ref, c16_ref, g1_ref, in_ref, s1, r1, l1, s2, r2, s3, r3):
        _peer_barrier("both")
        x, y, c = _place()
        start, mid, end = _gather_phases([in_ref], s3, r3)
        start_mod, mid_mod, end_mod = _gather_phases([g1_ref], s2, r2)
        _gather_blocks([blk_ref], [g0_ref], s1, r1, l1)
        start()
        c16 = jnp.concatenate([g0_ref[i, 0:1, :] for i in range(8)] + [cctx_ref[...], jnp.zeros((7, D), F32)],
                              axis=0)
        c16_ref[...] = c16
        g1_ref[4 * x + 2 * y + c] = _dot(c16 * _sig(c16), w_ref[...], prec=HI) + b_ref[...]
        start_mod()
        mid()
        mid_mod()
        end()
        end_mod()

    vm = pl.BlockSpec(memory_space=pltpu.VMEM)
    return pl.pallas_call(
        body, name="prologue",
        out_shape=[jax.ShapeDtypeStruct((8, 8, D), F32), jax.ShapeDtypeStruct((16, D), F32),
                   jax.ShapeDtypeStruct((8, 16, n), F32), jax.ShapeDtypeStruct(in8.shape, in8.dtype)],
        in_specs=[vm, vm, vm, vm, ANY], out_specs=[vm, vm, vm, ANY], input_output_aliases={4: 3},
        scratch_shapes=_gather_sems(1) + [pltpu.SemaphoreType.DMA((1, 7))] * 4,
        compiler_params=pltpu.CompilerParams(vmem_limit_bytes=48 << 20,
                                             collective_id=BARRIER_IDS["both"]))(blk, c_ctx, w, b, in8)


def _ada_bwd(c16, dmod16, w, carry=None):
    n = w.shape[1]
    tn = 512

    def body(c_ref, d_ref, w_ref, gw_ref, gc_ref):
        j = pl.program_id(0)

        @pl.when(j == 0)
        def _():
            gc_ref[...] = jnp.zeros_like(gc_ref)

        cc = c_ref[...]
        dm = d_ref[...]
        gw_ref[...] = _dot(cc * _sig(cc), dm, TN, prec=HI)
        gc_ref[...] += _dot(dm, w_ref[...], NT, prec=HI)

    return _pcall(body, name="ada_bwd", grid=(n // tn,),
                  in_specs=[_full((16, D)), pl.BlockSpec((16, tn), lambda j: (0, j)),
                            pl.BlockSpec((D, tn), lambda j: (0, j))],
                  out_specs=[pl.BlockSpec((D, tn), lambda j: (0, j)), _full((16, D))],
                  out_shape=[jax.ShapeDtypeStruct((D, n), F32),
                             jax.ShapeDtypeStruct((16, D), F32)], carry=carry)(c16, dmod16, w)


def _adam_math(w, g, m, v):
    c1 = 1.0 - ADAM_B1 ** ADAM_STEP
    c2 = 1.0 - ADAM_B2 ** ADAM_STEP
    nm = ADAM_B1 * m + (1.0 - ADAM_B1) * g
    nv = ADAM_B2 * v + (1.0 - ADAM_B2) * (g * g)
    return -ADAM_LR * ((nm / c1) / (jnp.sqrt(nv / c2) + ADAM_EPS) + ADAM_WD * w), nm, nv


def _adamw_small(ws, gs, ms, vs):
    n = len(ws)

    def body(*refs):
        for u in range(n):
            d_, nm, nv = _adam_math(refs[u][...], refs[n + u][...], refs[2 * n + u][...], refs[3 * n + u][...])
            refs[4 * n + u][...] = d_
            refs[5 * n + u][...] = nm
            refs[6 * n + u][...] = nv

    specs = [_full(w.shape) for w in ws]
    shapes = [jax.ShapeDtypeStruct(w.shape, F32) for w in ws]
    out = _pcall(body, name="adamw_small", grid=(1,), in_specs=specs * 4, out_specs=specs * 3,
                 out_shape=shapes * 3)(*ws, *gs, *ms, *vs)
    return out[:n], out[n:2 * n], out[2 * n:]


def _cctx_grad(parts, c_ctx):
    def body(p_ref, c_ref, o_ref):
        acc = p_ref[0:1, :]
        for k in range(1, 4):
            acc = acc + p_ref[k:k + 1, :]
        cc = c_ref[...]
        s = _sig(cc)
        o_ref[...] = acc * (s * (1.0 + cc * (1.0 - s)))

    return _pcall(body, name="cctx_grad", grid=(1,), in_specs=[_full(parts.shape), _full((1, D))],
                  out_specs=_full((1, D)), out_shape=jax.ShapeDtypeStruct((1, D), F32))(parts, c_ctx)


ADAM_STEPS = 8


def _adamw_multi(ws, gs, ms, vs, *, name):
    n = len(ws)

    def body(*refs):
        for u in range(n):
            g = refs[n + u][...]
            refs[4 * n + u][...] = g
            refs[5 * n + u][...], refs[6 * n + u][...], refs[7 * n + u][...] = _adam_math(
                refs[u][...], g, refs[2 * n + u][...], refs[3 * n + u][...])

    specs = [pl.BlockSpec((w.shape[0] // ADAM_STEPS, w.shape[1]), lambda i: (i, 0)) for w in ws]
    shapes = [jax.ShapeDtypeStruct(w.shape, F32) for w in ws]
    out = _pcall(body, name=name, grid=(ADAM_STEPS,), in_specs=specs * 4, out_specs=specs * 4,
                 out_shape=shapes * 4)(*ws, *gs, *ms, *vs)
    return out[:n], out[n:2 * n], out[2 * n:3 * n], out[3 * n:]


def kernel(x, c, ctx, c_ctx, w_ada, b_ada, norm_mix_w, norm_ffn_w, w_in, hgrn_lb_logits, hgrn_norm_w, q_norm_w, k_norm_w, attn_sinks, w_branch_hgrn, w_branch_attn, w_out, w_ffn_gate, w_ffn_up, w_ffn_down, loss_target, m_c_ctx, m_w_ada, m_b_ada, m_norm_mix_w, m_norm_ffn_w, m_w_in, m_hgrn_lb_logits, m_hgrn_norm_w, m_q_norm_w, m_k_norm_w, m_attn_sinks, m_w_branch_hgrn, m_w_branch_attn, m_w_out, m_w_ffn_gate, m_w_ffn_up, m_w_ffn_down, v_c_ctx, v_w_ada, v_b_ada, v_norm_mix_w, v_norm_ffn_w, v_w_in, v_hgrn_lb_logits, v_hgrn_norm_w, v_q_norm_w, v_k_norm_w, v_attn_sinks, v_w_branch_hgrn, v_w_branch_attn, v_w_out, v_w_ffn_gate, v_w_ffn_up, v_w_ffn_down):
    xi, yi, ci = _place()
    chip = 2 * xi + yi
    dev = 2 * chip + ci
    s_len = x.shape[1]

    shards = [w_in[0].T, w_branch_hgrn[0], w_branch_attn[0], w_out[0], w_ffn_gate[0].T, w_ffn_up[0].T,
              w_ffn_down[0]]
    bufs = _cast_place(shards, ci, dev)

    lbrow = jnp.pad(hgrn_lb_logits.reshape(1, 512), ((0, 0), (0, D - 512)))
    blk = jnp.concatenate([c, lbrow, jnp.zeros((6, D), F32)], axis=0)
    nada = w_ada.shape[2]
    b_sh = lax.dynamic_slice(b_ada, (0, chip * nada), (1, nada))
    g0, c16, g1, in8 = _prologue(blk, c_ctx[None], w_ada[0], b_sh, bufs[0])
    lg = g0[0::2, 1, :512].reshape(4, 2, 2, 128).transpose(1, 2, 0, 3).reshape(2, 2, HGW)
    modall = g1[0::2].transpose(1, 0, 2).reshape(16, 4 * nada)
    mod = lax.dynamic_slice(modall, (dev, 0), (1, 6 * D)).reshape(6, D)
    modc = modall[8].reshape(6, D)[:2]

    sq, gx, _, small, rs = _local_step(
        x[0], ctx[0], loss_target[0], mod, modc, norm_mix_w, norm_ffn_w, lg, hgrn_norm_w, q_norm_w,
        k_norm_w, attn_sinks[0], in8.reshape(NCOL, D), bufs[1:], dist=(ci, chip))

    def whole(r):
        return r.reshape(2 * r.shape[1], r.shape[2])

    g_dn, g_g, g_u = [whole(r) for r in rs["ffn_done"]]
    g_bh, g_ba, g_o = [whole(r) for r in rs["mix_done"]]
    in_pairs = rs["in_pairs"]

    g2, tot = _ag_small(small["raw"], sq)
    loss = 0.5 * jnp.sum(tot[24]) / D
    dmodc_tot = jnp.pad(tot[6:8].reshape(1, 2 * D), ((0, 0), (0, 4 * D)))
    g_b_ada = tot[0:6].reshape(1, 6 * D) + dmodc_tot
    dmod16 = jnp.concatenate([g2[:, 0:6].reshape(8, 6 * D), dmodc_tot, jnp.zeros((7, 6 * D), F32)], axis=0)
    (g_w_ada, gc_part), in_contribs = _ada_bwd(
        c16, lax.dynamic_slice(dmod16, (0, chip * nada), (16, nada)), w_ada[0],
        carry=_carry_chipx(in_pairs, rows=rs["rest_rows"], into=rs["in_part"]))
    in_reds, (g3,) = _rs_sibling_gather(_rs_chip_add(in_pairs, in_contribs, ci, chip), [gc_part[8:16]])
    g_in = whole(in_reds[0])
    g_c_ctx = _cctx_grad(g3[0::2, 0], c_ctx[None])[0]
    g_nw1 = tot[8:9]
    g_nw2 = tot[9:10]
    g_hw = tot[10, :HGW].reshape(4, HGD).sum(0, keepdims=True)
    g_qnw = tot[10, HGW:].reshape(8, HDIM).sum(0, keepdims=True)
    g_knw = tot[11, :128].reshape(2, HDIM).sum(0, keepdims=True)
    g_sinks = tot[16:24, 0][None]
    g_lg = lax.dynamic_slice(tot[12:16, :HGW].reshape(2, 2, HGW), (0, 0, chip * 128), (2, 2, 128))

    names = ["c_ctx", "w_ada", "b_ada", "norm_mix_w", "norm_ffn_w", "w_in", "hgrn_lb_logits", "hgrn_norm_w",
             "q_norm_w", "k_norm_w", "attn_sinks", "w_branch_hgrn", "w_branch_attn", "w_out", "w_ffn_gate",
             "w_ffn_up", "w_ffn_down"]
    ws = dict(zip(names, [c_ctx, w_ada, b_ada, norm_mix_w, norm_ffn_w, w_in, hgrn_lb_logits, hgrn_norm_w,
                          q_norm_w, k_norm_w, attn_sinks, w_branch_hgrn, w_branch_attn, w_out, w_ffn_gate,
                          w_ffn_up, w_ffn_down]))
    ms = dict(zip(names, [m_c_ctx, m_w_ada, m_b_ada, m_norm_mix_w, m_norm_ffn_w, m_w_in, m_hgrn_lb_logits,
                          m_hgrn_norm_w, m_q_norm_w, m_k_norm_w, m_attn_sinks, m_w_branch_hgrn,
                          m_w_branch_attn, m_w_out, m_w_ffn_gate, m_w_ffn_up, m_w_ffn_down]))
    vs = dict(zip(names, [v_c_ctx, v_w_ada, v_b_ada, v_norm_mix_w, v_norm_ffn_w, v_w_in, v_hgrn_lb_logits,
                          v_hgrn_norm_w, v_q_norm_w, v_k_norm_w, v_attn_sinks, v_w_branch_hgrn,
                          v_w_branch_attn, v_w_out, v_w_ffn_gate, v_w_ffn_up, v_w_ffn_down]))
    transposed = ("w_in", "w_ffn_gate", "w_ffn_up")

    def view(a, n):
        return a[0].T if n in transposed else a[0]

    def unview(a, n):
        return a.T[None] if n in transposed else a[None]

    delta, new_m, new_v, grads = {}, {}, {}, {}

    def big_adamw(group, gs, name):
        g_, d_, m_, v_ = _adamw_multi([view(ws[n], n) for n in group], gs, [view(ms[n], n) for n in group],
                                      [view(vs[n], n) for n in group], name=name)
        for i, n in enumerate(group):
            grads[n], delta[n], new_m[n], new_v[n] = (unview(g_[i], n), unview(d_[i], n), unview(m_[i], n),
                                                      unview(v_[i], n))

    big_adamw(["w_ffn_down", "w_ffn_gate", "w_ffn_up", "w_out", "w_branch_hgrn", "w_branch_attn"],
              [g_dn, g_g, g_u, g_o, g_bh, g_ba], "adamw_first")
    big_adamw(["w_in", "w_ada"], [g_in, g_w_ada], "adamw_second")
    grads.update(c_ctx=g_c_ctx, b_ada=g_b_ada, norm_mix_w=g_nw1, norm_ffn_w=g_nw2, hgrn_lb_logits=g_lg,
                 hgrn_norm_w=g_hw, q_norm_w=g_qnw, k_norm_w=g_knw, attn_sinks=g_sinks)
    small_names = [n for n in names if n not in delta]

    def two_d(a):
        return a.reshape(1, -1) if a.ndim == 1 else a

    sd, sm_, sv = _adamw_small(*[[two_d(d[n]) for n in small_names] for d in (ws, grads, ms, vs)])
    for i, n in enumerate(small_names):
        for dst, src in ((delta, sd), (new_m, sm_), (new_v, sv)):
            dst[n] = src[i].reshape(ws[n].shape)
    return (loss, gx[None], *[grads[n] for n in names], *[delta[n] for n in names],
            *[new_m[n] for n in names], *[new_v[n] for n in names])
```

```python
import functools

import numpy as np
import jax
import jax.numpy as jnp
from jax import lax
from jax.experimental import pallas as pl
from jax.experimental.pallas import tpu as pltpu

F32 = jnp.float32
BF16 = jnp.bfloat16
HI = lax.Precision.HIGHEST
MESH = pl.DeviceIdType.MESH

D = 1024
L = 256
TM = 256
HGW = 512
HGD = 128
CH = 32
ATW = 512
HDIM = 64
BLK = 128
GRID_W = 64
DFF = 2816
NCOL = 5376
EPS = 1e-6
ROPE_THETA = 10000.0
BF16_SUBLANES = 16

C_FB, C_INP, C_QHG, C_FF = 0, 1, 2, 3
C_GATES = 1
C_GHG, C_QRAW = 8, 9
C_KV = 20
C_QKV = 6

ADAM_LR, ADAM_B1, ADAM_B2, ADAM_EPS, ADAM_WD, ADAM_STEP = 0.001, 0.9, 0.999, 1e-08, 0.01, 10

NN = (((1,), (0,)), ((), ()))
NT = (((1,), (1,)), ((), ()))
TN = (((0,), (0,)), ((), ()))


def _dot(a, b, dims=NN, prec=None):
    return lax.dot_general(a, b, dims, precision=prec, preferred_element_type=F32)


def _bdot(a, b, dims=NN):
    return _dot(a.astype(BF16), b.astype(BF16), dims)


def _sig(x):
    return 1.0 / (1.0 + jnp.exp(-x))


class _Carry:
    def __init__(self, ins, outs, aliases, scratch, phases, peers):
        self.ins, self.outs, self.aliases, self.scratch, self.phases = ins, outs, aliases, scratch, phases
        self.peers = peers


BARRIER_IDS = {"sib": 1, "chips": 2, "both": 3}


def _peer_barrier(kind):
    x, y, c = _place()
    peers = []
    if kind in ("sib", "both"):
        peers.append((x, y, 1 - c))
    if kind in ("chips", "both"):
        peers += [(1 - x, y, c), (x, 1 - y, c), (1 - x, 1 - y, c)]
    bar = pltpu.get_barrier_semaphore()
    for peer in peers:
        pl.semaphore_signal(bar, inc=1, device_id=peer, device_id_type=MESH)
    pl.semaphore_wait(bar, len(peers))


def _in_hbm(args):
    return [pltpu.with_memory_space_constraint(a, pltpu.HBM) for a in args]


def _out_hbm(shapes):
    if isinstance(shapes, (list, tuple)):
        return [pltpu.HBM(s.shape, s.dtype) for s in shapes]
    return pltpu.HBM(shapes.shape, shapes.dtype)


def _carry_join(a, b):
    na_in, na_out, na_sc = len(a.ins), len(a.outs), len(a.scratch)
    aliases = dict(a.aliases)
    aliases.update({na_in + i: na_out + o for i, o in b.aliases.items()})

    def phases(ins, outs, sems):
        pa = a.phases(ins[:na_in], outs[:na_out], sems[:na_sc])
        pb = b.phases(ins[na_in:], outs[na_out:], sems[na_sc:])

        def both(fa, fb):
            if fa is None and fb is None:
                return None

            def run():
                for fn in (fa, fb):
                    if fn is not None:
                        fn()
            return run

        return tuple(both(fa, fb) for fa, fb in zip(pa, pb))

    return _Carry(list(a.ins) + list(b.ins), list(a.outs) + list(b.outs), aliases,
                  list(a.scratch) + list(b.scratch), phases, a.peers if a.peers == b.peers else "both")


def _pcall(body, *, name, grid, in_specs, out_specs, out_shape, scratch=(), aliases=None, vmem_mb=48,
           carry=None):
    params = pltpu.CompilerParams(dimension_semantics=("arbitrary",) * len(grid),
                                  vmem_limit_bytes=vmem_mb << 20)
    if carry is None:
        plain = pl.pallas_call(
            body, name=name, grid=grid, in_specs=in_specs, out_specs=out_specs, out_shape=_out_hbm(out_shape),
            scratch_shapes=list(scratch), input_output_aliases=aliases or {}, compiler_params=params)
        return lambda *args: plain(*_in_hbm(args))
    single = not isinstance(out_shape, (list, tuple))
    out_specs_l = [out_specs] if single else list(out_specs)
    out_shape_l = [out_shape] if single else list(out_shape)
    n_in, n_out, n_sc = len(in_specs), len(out_shape_l), len(scratch)
    k_in, k_out = len(carry.ins), len(carry.outs)
    nsteps = int(np.prod(grid))
    assert nsteps >= 3

    def wrapped(*refs):
        ins, cins = refs[:n_in], refs[n_in:n_in + k_in]
        o0 = n_in + k_in
        outs, couts = refs[o0:o0 + n_out], refs[o0 + n_out:o0 + n_out + k_out]
        s0 = o0 + n_out + k_out
        sc, csc = refs[s0:s0 + n_sc], refs[s0 + n_sc:]
        step = pl.program_id(0)
        for ax in range(1, len(grid)):
            step = step * grid[ax] + pl.program_id(ax)
        start, mid, end = carry.phases(cins, couts, csc)

        @pl.when(step == 0)
        def _():
            _peer_barrier(carry.peers)
            start()

        body(*ins, *outs, *sc)
        if mid is not None:
            pl.when(step == nsteps - 2)(mid)
        pl.when(step == nsteps - 1)(end)

    all_aliases = dict(aliases or {})
    all_aliases.update({n_in + i: n_out + o for i, o in carry.aliases.items()})
    call = pl.pallas_call(
        wrapped, name=name, grid=grid, in_specs=list(in_specs) + [ANY] * k_in,
        out_specs=out_specs_l + [ANY] * k_out, out_shape=_out_hbm(out_shape_l + list(carry.outs)),
        scratch_shapes=list(scratch) + list(carry.scratch), input_output_aliases=all_aliases,
        compiler_params=pltpu.CompilerParams(dimension_semantics=("arbitrary",) * len(grid),
                                             vmem_limit_bytes=vmem_mb << 20,
                                             collective_id=BARRIER_IDS[carry.peers]))

    def run(*args):
        res = call(*_in_hbm(args), *_in_hbm(carry.ins))
        core = res[:n_out]
        return (core[0] if single else list(core)), list(res[n_out:])

    return run


def _full(shape):
    nd = len(shape)
    return pl.BlockSpec(shape, lambda *_: (0,) * nd)


ANY = pl.BlockSpec(memory_space=pl.ANY)


NT_IN = NCOL // 256


def _src_block(j):
    return j + jnp.where(j < 4, 2, jnp.where(j < 6, 3, jnp.where(j < 8, -6, jnp.where(
        j < 16, 5, jnp.where(j < 20, -7, -14)))))


def _mm_in(h, wt, tm, carry=None):
    tt = h.shape[0]

    def body(h_ref, w_ref, o_ref):
        o_ref[...] = _bdot(h_ref[...], w_ref[...], NT)

    return _pcall(body, name="mm_in", grid=(tt // tm, NT_IN),
                  in_specs=[pl.BlockSpec((tm, D), lambda i, j: (i, 0)),
                            pl.BlockSpec((256, D), lambda i, j: (_src_block(j), 0))],
                  out_specs=pl.BlockSpec((tm, 256), lambda i, j: (i, j)),
                  out_shape=jax.ShapeDtypeStruct((tt, NCOL), F32), carry=carry)(h, wt)


def _mm_dh(dp, wt, tm, carry=None):
    tt = dp.shape[0]
    per, ng = 3, NT_IN // 3

    def body(d_ref, w0, w1, w2, o_ref, acc):
        kk = pl.program_id(1)

        @pl.when(kk == 0)
        def _():
            acc[...] = jnp.zeros_like(acc)

        acc[...] += (_bdot(d_ref[:, 0:256], w0[...]) + _bdot(d_ref[:, 256:512], w1[...])
                     + _bdot(d_ref[:, 512:768], w2[...]))

        @pl.when(kk == ng - 1)
        def _():
            o_ref[...] = acc[...]

    wspecs = [pl.BlockSpec((256, D), functools.partial(lambda t, i, kk: (_src_block(per * kk + t), 0), t))
              for t in range(per)]
    return _pcall(body, name="mm_dh", grid=(tt // tm, ng),
                  in_specs=[pl.BlockSpec((tm, per * 256), lambda i, kk: (i, kk))] + wspecs,
                  out_specs=pl.BlockSpec((tm, D), lambda i, kk: (i, 0)),
                  out_shape=jax.ShapeDtypeStruct((tt, D), F32), scratch=[pltpu.VMEM((tm, D), F32)],
                  carry=carry)(dp, wt, wt, wt)


def _mm_gin(dp, h, tk):
    tt = dp.shape[0]
    nk = tt // tk

    def body(d_ref, h_ref, o_ref, acc):
        kk = pl.program_id(1)

        @pl.when(kk == 0)
        def _():
            acc[...] = jnp.zeros_like(acc)

        acc[...] += _bdot(d_ref[...], h_ref[...], TN)

        @pl.when(kk == nk - 1)
        def _():
            o_ref[...] = acc[...].astype(BF16)

    return _pcall(body, name="mm_gin", grid=(NT_IN, nk),
                  in_specs=[pl.BlockSpec((tk, 256), lambda j, kk: (kk, j)),
                            pl.BlockSpec((tk, D), lambda j, kk: (kk, 0))],
                  out_specs=pl.BlockSpec((256, D), lambda j, kk: (_src_block(j), 0)),
                  out_shape=jax.ShapeDtypeStruct((NCOL, D), BF16), scratch=[pltpu.VMEM((256, D), F32)])(dp, h)


def _tok_specs():
    assert L == TM
    return [_full((TM, D)), pl.BlockSpec((TM, D), lambda i: (jnp.maximum(i - 1, 0), 0))]


def _mod1(ctx, x, nw, ss):
    rows = L + x.shape[0]

    def body(c_ref, x_ref, nw_ref, ss_ref, h_ref):
        t = jnp.where(pl.program_id(0) == 0, c_ref[...], x_ref[...])
        r = lax.rsqrt(jnp.mean(t * t, axis=-1, keepdims=True) + EPS)
        s = ss_ref[0]
        h_ref[...] = ((t * r * nw_ref[...]) * (1.0 + s[1:2]) + s[0:1]).astype(BF16)

    return _pcall(body, name="mod1", grid=(rows // TM,),
                  in_specs=_tok_specs() + [_full((1, D)),
                                           pl.BlockSpec((1, 2, D), lambda i: (jnp.minimum(i, 1), 0, 0))],
                  out_specs=pl.BlockSpec((TM, D), lambda i: (i, 0)),
                  out_shape=jax.ShapeDtypeStruct((rows, D), BF16))(ctx, x, nw, ss)


def _norm_bwd_rows(x, dh, nw, scale):
    r = lax.rsqrt(jnp.mean(x * x, axis=-1, keepdims=True) + EPS)
    xh = x * r
    dxh = dh * ((1.0 + scale) * nw)
    dx = r * (dxh - xh * jnp.mean(dxh * xh, axis=-1, keepdims=True))
    return dx, xh


def _out_proj_mod2(mixed, w_o, x, g1, nw2, ss2):
    s_len = x.shape[0]
    tm = 512

    def body(m_ref, w_ref, x_ref, g_ref, nw_ref, ss_ref, ao_ref, x1_ref, h_ref):
        ao = _bdot(m_ref[...], w_ref[...])
        ao_ref[...] = ao.astype(BF16)
        x1 = x_ref[...] + g_ref[...] * ao
        x1_ref[...] = x1
        r = lax.rsqrt(jnp.mean(x1 * x1, axis=-1, keepdims=True) + EPS)
        s = ss_ref[0]
        h_ref[...] = ((x1 * r * nw_ref[...]) * (1.0 + s[1:2]) + s[0:1]).astype(BF16)

    row = pl.BlockSpec((tm, D), lambda i: (i, 0))
    f = jax.ShapeDtypeStruct((s_len, D), F32)
    return _pcall(body, name="out_proj_mod2", grid=(s_len // tm,),
                  in_specs=[row, _full((D, D)), row, _full((1, D)), _full((1, D)), _full((1, 2, D))],
                  out_specs=[row, row, row],
                  out_shape=[jax.ShapeDtypeStruct((s_len, D), BF16), f,
                             jax.ShapeDtypeStruct((s_len, D), BF16)])(mixed, w_o, x, g1, nw2, ss2)


TS = 1024


def _acc_call(body, *, name, grid, in_specs, out_specs, out_shape, acc_shapes, args, carry=None):
    return _pcall(body, name=name, grid=grid, in_specs=in_specs, out_specs=out_specs, out_shape=out_shape,
                  scratch=[pltpu.VMEM(s, F32) for s in acc_shapes], carry=carry)(*args)


def _ffn_up(h2, g4, u4, carry=None):
    s_len = h2.shape[0]
    ns = g4.shape[1]

    def body(h_ref, g_ref, u_ref, a_ref, b_ref, z_ref):
        h = h_ref[...]
        a = _bdot(h, g_ref[0], NT)
        b = _bdot(h, u_ref[0], NT)
        a_ref[0] = a.astype(BF16)
        b_ref[0] = b.astype(BF16)
        z_ref[0] = (a * _sig(a) * b).astype(BF16)

    w = pl.BlockSpec((1, ns, D), lambda i, j: (j, 0, 0))
    o = pl.BlockSpec((1, TS, ns), lambda i, j: (j, i, 0))
    f = jax.ShapeDtypeStruct((4, s_len, ns), BF16)
    return _pcall(body, name="ffn_up", grid=(s_len // TS, 4),
                  in_specs=[pl.BlockSpec((TS, D), lambda i, j: (i, 0)), w, w], out_specs=[o, o, o],
                  out_shape=[f, f, jax.ShapeDtypeStruct((4, s_len, ns), BF16)], carry=carry)(h2, g4, u4)


def _ffn_down_loss(z4, dn4, x1, g2, tgt):
    _, s_len, ns = z4.shape

    def body(z_ref, w_ref, x1_ref, g_ref, t_ref, sq_ref, dx2_ref, dyb_ref, dg_ref, acc):
        i, j = pl.program_id(0), pl.program_id(1)

        @pl.when((i == 0) & (j == 0))
        def _():
            sq_ref[...] = jnp.zeros_like(sq_ref)
            dg_ref[...] = jnp.zeros_like(dg_ref)

        @pl.when(j == 0)
        def _():
            acc[...] = jnp.zeros_like(acc)

        acc[...] += _bdot(z_ref[0], w_ref[0])

        @pl.when(j == 3)
        def _():
            y_ = acc[...]
            g = g_ref[...]
            e = x1_ref[...] + g * y_ - t_ref[...]
            sq_ref[...] += jnp.sum(e * e, axis=0, keepdims=True)
            dx2 = e * (1.0 / D)
            dx2_ref[...] = dx2
            dyb_ref[...] = (g * dx2).astype(BF16)
            dg_ref[...] += jnp.sum(dx2 * y_, axis=0, keepdims=True)

    row = pl.BlockSpec((TS, D), lambda i, j: (i, 0))
    vec = _full((1, D))
    return _acc_call(body, name="ffn_down_loss", grid=(s_len // TS, 4),
                     in_specs=[pl.BlockSpec((1, TS, ns), lambda i, j: (j, i, 0)),
                               pl.BlockSpec((1, ns, D), lambda i, j: (j, 0, 0)), row, vec, row],
                     out_specs=[vec, row, row, vec],
                     out_shape=[jax.ShapeDtypeStruct((1, D), F32), jax.ShapeDtypeStruct((s_len, D), F32),
                                jax.ShapeDtypeStruct((s_len, D), BF16), jax.ShapeDtypeStruct((1, D), F32)],
                     acc_shapes=[(TS, D)], args=(z4, dn4, x1, g2, tgt))


def _ffn_dz(dyb, dn4, a4, b4):
    _, s_len, ns = a4.shape

    def body(dy_ref, w_ref, a_ref, b_ref, da_ref, db_ref):
        dz = _bdot(dy_ref[...], w_ref[0], NT)
        a = a_ref[0].astype(F32)
        s = _sig(a)
        da_ref[0] = (dz * b_ref[0].astype(F32) * (s * (1.0 + a * (1.0 - s)))).astype(BF16)
        db_ref[0] = (dz * (a * s)).astype(BF16)

    t = pl.BlockSpec((1, TS, ns), lambda i, j: (j, i, 0))
    o = jax.ShapeDtypeStruct((4, s_len, ns), BF16)
    return _pcall(body, name="ffn_dz", grid=(s_len // TS, 4),
                  in_specs=[pl.BlockSpec((TS, D), lambda i, j: (i, 0)),
                            pl.BlockSpec((1, ns, D), lambda i, j: (j, 0, 0)), t, t],
                  out_specs=[t, t], out_shape=[o, o])(dyb, dn4, a4, b4)


def _ffn_gdn(z4, dyb):
    _, s_len, ns = z4.shape
    tk = min(s_len, 2 * TS)
    nk = s_len // tk

    def body(z_ref, dy_ref, o_ref, acc):
        t = pl.program_id(1)

        @pl.when(t == 0)
        def _():
            acc[...] = jnp.zeros_like(acc)

        acc[...] += _bdot(z_ref[0], dy_ref[...], TN)

        @pl.when(t == nk - 1)
        def _():
            o_ref[0] = acc[...].astype(o_ref.dtype)

    return _acc_call(body, name="ffn_gdn", grid=(4, nk),
                     in_specs=[pl.BlockSpec((1, tk, ns), lambda j, t: (j, t, 0)),
                               pl.BlockSpec((tk, D), lambda j, t: (t, 0))],
                     out_specs=pl.BlockSpec((1, ns, D), lambda j, t: (j, 0, 0)),
                     out_shape=jax.ShapeDtypeStruct((4, ns, D), BF16), acc_shapes=[(ns, D)], args=(z4, dyb))


def _ffn_dh2(da4, db4, g4, u4, carry=None):
    _, s_len, ns = da4.shape

    def body(da_ref, db_ref, g_ref, u_ref, o_ref, acc):
        j = pl.program_id(1)

        @pl.when(j == 0)
        def _():
            acc[...] = jnp.zeros_like(acc)

        acc[...] += _bdot(da_ref[0], g_ref[0]) + _bdot(db_ref[0], u_ref[0])

        @pl.when(j == 3)
        def _():
            o_ref[...] = acc[...]

    t = pl.BlockSpec((1, TS, ns), lambda i, j: (j, i, 0))
    w = pl.BlockSpec((1, ns, D), lambda i, j: (j, 0, 0))
    return _acc_call(body, name="ffn_dh2", grid=(s_len // TS, 4), in_specs=[t, t, w, w],
                     out_specs=pl.BlockSpec((TS, D), lambda i, j: (i, 0)),
                     out_shape=jax.ShapeDtypeStruct((s_len, D), F32), acc_shapes=[(TS, D)],
                     args=(da4, db4, g4, u4), carry=carry)


def _ffn_ggu(h2, da4, db4, carry=None):
    _, s_len, ns = da4.shape
    nk = s_len // TS

    def body(h_ref, da_ref, db_ref, gg_ref, gu_ref, acc_g, acc_u):
        t = pl.program_id(1)

        @pl.when(t == 0)
        def _():
            acc_g[...] = jnp.zeros_like(acc_g)
            acc_u[...] = jnp.zeros_like(acc_u)

        h = h_ref[...]
        acc_g[...] += _bdot(da_ref[0], h, TN)
        acc_u[...] += _bdot(db_ref[0], h, TN)

        @pl.when(t == nk - 1)
        def _():
            gg_ref[0] = acc_g[...].astype(BF16)
            gu_ref[0] = acc_u[...].astype(BF16)

    d = pl.BlockSpec((1, TS, ns), lambda j, t: (j, t, 0))
    o = pl.BlockSpec((1, ns, D), lambda j, t: (j, 0, 0))
    f = jax.ShapeDtypeStruct((4, ns, D), BF16)
    return _acc_call(body, name="ffn_ggu", grid=(4, nk),
                     in_specs=[pl.BlockSpec((TS, D), lambda j, t: (t, 0)), d, d], out_specs=[o, o],
                     out_shape=[f, f], acc_shapes=[(ns, D), (ns, D)], args=(h2, da4, db4), carry=carry)


def _mod2_bwd(x1, dh2, dx2, ao, nw2, ss2, g1):
    s_len = x1.shape[0]

    def body(x1_ref, dh_ref, dx2_ref, ao_ref, nw_ref, ss_ref, g_ref,
             dx1_ref, da_ref, dss_ref, dnw_ref, dg_ref):
        i = pl.program_id(0)

        @pl.when(i == 0)
        def _():
            dss_ref[...] = jnp.zeros_like(dss_ref)
            dnw_ref[...] = jnp.zeros_like(dnw_ref)
            dg_ref[...] = jnp.zeros_like(dg_ref)

        dh = dh_ref[...]
        nw = nw_ref[...]
        scale = ss_ref[0][1:2]
        dxn, xh = _norm_bwd_rows(x1_ref[...], dh, nw, scale)
        dx1 = dx2_ref[...] + dxn
        dx1_ref[...] = dx1
        da_ref[...] = (g_ref[...] * dx1).astype(BF16)
        dg_ref[...] += jnp.sum(dx1 * ao_ref[...].astype(F32), axis=0, keepdims=True)
        dsh = jnp.sum(dh, axis=0, keepdims=True)
        dsc = jnp.sum(dh * xh * nw, axis=0, keepdims=True)
        dss_ref[...] += jnp.concatenate([dsh, dsc], axis=0)
        dnw_ref[...] += jnp.sum(dh * xh * (1.0 + scale), axis=0, keepdims=True)

    row = pl.BlockSpec((TM, D), lambda i: (i, 0))
    vec = _full((1, D))
    return _pcall(body, name="mod2_bwd", grid=(s_len // TM,),
                  in_specs=[row, row, row, row, vec, _full((1, 2, D)), vec],
                  out_specs=[row, row, _full((2, D)), vec, vec],
                  out_shape=[jax.ShapeDtypeStruct((s_len, D), F32), jax.ShapeDtypeStruct((s_len, D), BF16),
                             jax.ShapeDtypeStruct((2, D), F32), jax.ShapeDtypeStruct((1, D), F32),
                             jax.ShapeDtypeStruct((1, D), F32)])(x1, dh2, dx2, ao, nw2, ss2, g1)


def _mod1_bwd(ctx, x, dh, dx1, nw1, ss1):
    s_len = dx1.shape[0]
    tt = L + s_len

    def body(c_ref, x_ref, dh_ref, dx1_ref, nw_ref, ss_ref, dx_ref, dss_ref, dnw_ref):
        i = pl.program_id(0)
        tok = jnp.where(i == 0, c_ref[...], x_ref[...])

        @pl.when(i == 0)
        def _():
            dnw_ref[...] = jnp.zeros_like(dnw_ref)

        @pl.when(i <= 1)
        def _():
            dss_ref[...] = jnp.zeros_like(dss_ref)

        dh_ = dh_ref[...]
        nw = nw_ref[...]
        scale = ss_ref[0][1:2]
        dxn, xh = _norm_bwd_rows(tok, dh_, nw, scale)

        @pl.when(i >= 1)
        def _():
            dx_ref[...] = dx1_ref[...] + dxn

        dsh = jnp.sum(dh_, axis=0, keepdims=True)
        dsc = jnp.sum(dh_ * xh * nw, axis=0, keepdims=True)
        dss_ref[...] += jnp.concatenate([dsh, dsc], axis=0)[None]
        dnw_ref[...] += jnp.sum(dh_ * xh * (1.0 + scale), axis=0, keepdims=True)

    row = pl.BlockSpec((TM, D), lambda i: (i, 0))
    lat = pl.BlockSpec((TM, D), lambda i: (jnp.maximum(i - 1, 0), 0))
    sel = pl.BlockSpec((1, 2, D), lambda i: (jnp.minimum(i, 1), 0, 0))
    return _pcall(body, name="mod1_bwd", grid=(tt // TM,),
                  in_specs=_tok_specs() + [row, lat, _full((1, D)), sel],
                  out_specs=[lat, sel, _full((1, D))],
                  out_shape=[jax.ShapeDtypeStruct((s_len, D), F32), jax.ShapeDtypeStruct((2, 2, D), F32),
                             jax.ShapeDtypeStruct((1, D), F32)])(ctx, x, dh, dx1, nw1, ss1)


def _rows(c):
    return slice(c * CH, (c + 1) * CH)


def _chunk_masks(rev, transpose=False):
    r = lax.broadcasted_iota(jnp.int32, (TM, TM), 0)
    c = lax.broadcasted_iota(jnp.int32, (TM, TM), 1)
    same = (r // CH) == (c // CH)
    before = (c >= r) if (rev != transpose) else (c <= r)
    return same & before, same


def _chunk_scan(x, rev, transpose=False):
    r = lax.broadcasted_iota(jnp.int32, (CH, CH), 0)
    c = lax.broadcasted_iota(jnp.int32, (CH, CH), 1)
    tri = ((c >= r) if (rev != transpose) else (c <= r)).astype(F32)
    return jnp.concatenate([_dot(tri, x[_rows(ch)], prec=HI) for ch in range(x.shape[0] // CH)], axis=0)


def _chunk_total(x):
    return jnp.concatenate([jnp.broadcast_to(jnp.sum(x[_rows(ch)], axis=0, keepdims=True), (CH, x.shape[1]))
                            for ch in range(x.shape[0] // CH)], axis=0)


def _hgrn_gate(fl, qraw, lg):
    lb = 1.0 / (1.0 + jnp.exp(lg[1:2] - lg[0:1]))
    sg = _sig(fl)
    f = lb + (1.0 - lb) * sg
    q = qraw * _sig(qraw) * (HGD ** -0.5)
    return lb, sg, f, q


def _hgrn_fwd(p, lg, *, rev, carry=None, readout=None):
    tt = p.shape[0]
    nt = tt // TM
    ncht = TM // CH
    d = 1 if rev else 0

    def tile_of(s):
        return jnp.where(s == 0, 0, nt - s) if rev else s

    def body(*refs):
        if readout is None:
            f_ref, inp_ref, q_ref, lg_ref, o_ref, st_ref, state = refs
        else:
            f_ref, inp_ref, q_ref, lg_ref, oo_ref, g_ref, hw_ref, o_ref, st_ref, y_ref, state = refs
        s = pl.program_id(0)

        @pl.when(s == 0)
        def _():
            state[...] = jnp.zeros_like(state)

        _, _, f, q = _hgrn_gate(f_ref[...], q_ref[...], lg_ref[0])
        lf = jnp.log(f)
        causal, _ = _chunk_masks(rev)
        cum = _chunk_scan(lf, rev)
        tot = _chunk_total(lf)
        qd = (q * jnp.exp(cum)).astype(BF16)
        kd = ((1.0 - f) * jnp.exp(-cum)).astype(BF16)
        ke = ((1.0 - f) * jnp.exp(tot - cum)).astype(BF16)
        et = jnp.exp(tot)
        v = inp_ref[...].astype(BF16)
        order = range(ncht - 1, -1, -1) if rev else range(ncht)
        outs = []
        for h in range(4):
            sl = slice(h * HGD, (h + 1) * HGD)
            qd_, kd_, ke_, v_ = qd[:, sl], kd[:, sl], ke[:, sl], v[:, sl]
            pm = jnp.where(causal, _dot(qd_, kd_, NT), 0.0).astype(BF16)
            o_h = _dot(pm, v_)
            upd = [_dot(v_[_rows(c)], ke_[_rows(c)], TN) for c in range(ncht)]
            st = state[h]
            for c in order:
                st_ref[c, h] = st
                st = st * et[c * CH:c * CH + 1, sl] + upd[c]
            state[h] = st
            inter = [_dot(qd_[_rows(c)], st_ref[c, h].astype(BF16), NT) for c in range(ncht)]
            outs.append(o_h + jnp.concatenate(inter, axis=0))
        o_tile = jnp.concatenate(outs, axis=1)
        o_ref[...] = o_tile
        if readout is not None:
            @pl.when(tile_of(s) >= 1)
            def _():
                g = g_ref[...]
                y_ref[...] = (_head_rms(oo_ref[...] + o_tile, None, 4) * hw_ref[...] * (g * _sig(g))).astype(BF16)

    def col(cb):
        return pl.BlockSpec((TM, HGW), lambda s: (tile_of(s), cb))

    in_specs = [col(C_FB if rev else C_FF), col(C_INP), col(C_QHG), pl.BlockSpec((1, 2, HGW), lambda s: (d, 0, 0))]
    out_specs = [col(0), pl.BlockSpec((ncht, 4, HGD, HGD), lambda s: (tile_of(s), 0, 0, 0))]
    out_shape = [jax.ShapeDtypeStruct((tt, HGW), F32), jax.ShapeDtypeStruct((nt * ncht, 4, HGD, HGD), F32)]
    args = [p, p, p, lg]
    if readout is not None:
        in_specs += [col(0), col(C_GHG), _full((1, HGW))]
        args += [readout[0], p, readout[1]]
        assert rev
        out_specs.append(pl.BlockSpec((TM, HGW), lambda s: (jnp.where(s == 0, nt - 2, tile_of(s) - 1), 0)))
        out_shape.append(jax.ShapeDtypeStruct((tt - L, HGW), BF16))
    return _pcall(body, name="hgrn_fwd_rev" if rev else "hgrn_fwd", grid=(nt,), in_specs=in_specs,
                  out_specs=out_specs, out_shape=out_shape, scratch=[pltpu.VMEM((4, HGD, HGD), F32)],
                  carry=carry)(*args)


def _hgrn_bwd(p, lg, do, st, dp, prev, *, rev, carry=None):
    tt = p.shape[0]
    nt = tt // TM
    ncht = TM // CH
    d = 1 if rev else 0
    second = prev is not None

    def tile_of(s):
        return jnp.where(s == nt - 1, 0, s + 1) if rev else nt - 1 - s

    def body(*refs):
        if second:
            (f_ref, inp_ref, q_ref, lg_ref, do_ref, st_ref, dvp_ref, dqp_ref, _dp_in,
             dp_ref, dlg_ref, dstate) = refs
        else:
            (f_ref, inp_ref, q_ref, lg_ref, do_ref, st_ref, _dp_in,
             dp_ref, dv_ref, dq_ref, dlg_ref, dstate) = refs
        s = pl.program_id(0)
        tile = tile_of(s)

        @pl.when(s == 0)
        def _():
            dstate[...] = jnp.zeros_like(dstate)
            dlg_ref[...] = jnp.zeros_like(dlg_ref)

        qraw = q_ref[...]
        lb, sg, f, q = _hgrn_gate(f_ref[...], qraw, lg_ref[0])
        lf = jnp.log(f)
        causal, _ = _chunk_masks(rev)
        causal_t, _ = _chunk_masks(rev, transpose=True)
        cum = _chunk_scan(lf, rev)
        tot = _chunk_total(lf)
        ea, eb, ee, et = jnp.exp(cum), jnp.exp(-cum), jnp.exp(tot - cum), jnp.exp(tot)
        qdf, kdf, kef = q * ea, (1.0 - f) * eb, (1.0 - f) * ee
        qd, kd, ke = qdf.astype(BF16), kdf.astype(BF16), kef.astype(BF16)
        v = inp_ref[...].astype(BF16)
        dob = jnp.where(tile == 0, 0.0, do_ref[...]).astype(BF16)
        order = range(ncht) if rev else range(ncht - 1, -1, -1)
        dq_l, dk_l, dv_l, dcum_l, dtot_l = [], [], [], [], []
        for h in range(4):
            sl = slice(h * HGD, (h + 1) * HGD)
            qd_, kd_, ke_, v_, do_ = qd[:, sl], kd[:, sl], ke[:, sl], v[:, sl], dob[:, sl]
            pmt = jnp.where(causal_t, _dot(kd_, qd_, NT), 0.0).astype(BF16)
            dpm = jnp.where(causal, _dot(do_, v_, NT), 0.0).astype(BF16)
            dpmt = jnp.where(causal_t, _dot(v_, do_, NT), 0.0).astype(BF16)
            dv = _dot(pmt, do_)
            dqd = _dot(dpm, kd_)
            dkd = _dot(dpmt, qd_)
            upd = [_dot(do_[_rows(c)], qd_[_rows(c)], TN) for c in range(ncht)]
            ds = dstate[h]
            ds1 = [None] * ncht
            for c in order:
                ds1[c] = ds
                ds = ds * et[c * CH:c * CH + 1, sl] + upd[c]
            dstate[h] = ds
            dke_c, dv_c, dqd_c, dtot_c = [], [], [], []
            for c in range(ncht):
                st0 = st_ref[c, h]
                dsb = ds1[c].astype(BF16)
                dke_ = _dot(v_[_rows(c)], dsb)
                dke_c.append(dke_)
                dv_c.append(_dot(ke_[_rows(c)], dsb, NT))
                dqd_c.append(_dot(do_[_rows(c)], st0.astype(BF16)))
                dt = (jnp.sum(ds1[c] * st0, axis=0, keepdims=True) * et[c * CH:c * CH + 1, sl]
                      + jnp.sum(dke_ * kef[_rows(c), sl], axis=0, keepdims=True))
                dtot_c.append(jnp.broadcast_to(dt, (CH, HGD)))
            dke = jnp.concatenate(dke_c, axis=0)
            dqd = dqd + jnp.concatenate(dqd_c, axis=0)
            dv_l.append(dv + jnp.concatenate(dv_c, axis=0))
            dtot_l.append(jnp.concatenate(dtot_c, axis=0))
            dq_l.append(dqd * ea[:, sl])
            dk_l.append(dkd * eb[:, sl] + dke * ee[:, sl])
            dcum_l.append(dqd * qdf[:, sl] - dkd * kdf[:, sl] - dke * kef[:, sl])
        dcum = jnp.concatenate(dcum_l, axis=1)
        dlf = _chunk_scan(dcum, rev, transpose=True) + jnp.concatenate(dtot_l, axis=1)
        dq_t = jnp.concatenate(dq_l, axis=1)
        dv_t = jnp.concatenate(dv_l, axis=1)

        df = dlf / f - jnp.concatenate(dk_l, axis=1)
        dfl = df * (1.0 - lb) * sg * (1.0 - sg)
        dlb = jnp.sum(df * (1.0 - sg), axis=0, keepdims=True)
        dl0 = dlb * lb * (1.0 - lb)
        dlg_ref[...] += jnp.concatenate([dl0, -dl0], axis=0)[None]
        if second:
            sq = _sig(qraw)
            dqr = (dqp_ref[...] + dq_t) * (HGD ** -0.5) * (sq * (1.0 + qraw * (1.0 - sq)))
            dp_ref[...] = jnp.concatenate([dfl, dvp_ref[...] + dv_t, dqr], axis=1).astype(BF16)
        else:
            dp_ref[...] = dfl.astype(BF16)
            dv_ref[...] = dv_t
            dq_ref[...] = dq_t

    def col(cb):
        return pl.BlockSpec((TM, HGW), lambda s: (tile_of(s), cb))

    tok = pl.BlockSpec((TM, HGW), lambda s: (tile_of(s), 0))
    in_specs = [col(C_FB if rev else C_FF), col(C_INP), col(C_QHG),
                pl.BlockSpec((1, 2, HGW), lambda s: (d, 0, 0)),
                pl.BlockSpec((TM, HGW), lambda s: (jnp.maximum(tile_of(s) - 1, 0), 0)),
                pl.BlockSpec((ncht, 4, HGD, HGD), lambda s: (tile_of(s), 0, 0, 0))]
    args = [p, p, p, lg, do, st]
    dlg_spec = _full((1, 2, HGW))
    dlg_shape = jax.ShapeDtypeStruct((1, 2, HGW), F32)
    if second:
        in_specs += [tok, tok]
        args += [prev[0], prev[1]]
        out_specs = [pl.BlockSpec((TM, 3 * HGW), lambda s: (tile_of(s), 0)), dlg_spec]
        out_shape = [jax.ShapeDtypeStruct(dp.shape, BF16), dlg_shape]
    else:
        out_specs = [pl.BlockSpec((TM, HGW), lambda s: (tile_of(s), C_FB if rev else C_FF)), tok, tok, dlg_spec]
        out_shape = [jax.ShapeDtypeStruct(dp.shape, BF16), jax.ShapeDtypeStruct((tt, HGW), F32),
                     jax.ShapeDtypeStruct((tt, HGW), F32), dlg_shape]
    in_specs.append(ANY)
    args.append(dp)
    return _pcall(body, name="hgrn_bwd_rev" if rev else "hgrn_bwd", grid=(nt,),
                  in_specs=in_specs, out_specs=out_specs, out_shape=out_shape,
                  scratch=[pltpu.VMEM((4, HGD, HGD), F32)],
                  aliases={len(args) - 1: 0}, carry=carry)(*args)


def _head_rms(o, w, nheads):
    outs = []
    for h in range(nheads):
        oh = o[:, h * HGD:(h + 1) * HGD]
        outs.append(oh * lax.rsqrt(jnp.mean(oh * oh, axis=-1, keepdims=True) + EPS))
    return jnp.concatenate(outs, axis=1)


def _readout_bwd(o0, o1, p, hw4, dy, dp, carry=None):
    tt = o0.shape[0]
    s_len = tt - L

    def body(o0_ref, o1_ref, g_ref, w_ref, dy_ref, _dp_in, dp_ref, do_ref, dw_ref):
        i = pl.program_id(0)

        @pl.when(i == 0)
        def _():
            dw_ref[...] = jnp.zeros_like(dw_ref)
            dp_ref[...] = jnp.zeros_like(dp_ref)

        @pl.when(i >= 1)
        def _():
            o = o0_ref[...] + o1_ref[...]
            g = g_ref[...]
            w = w_ref[...]
            sg = _sig(g)
            dy_ = dy_ref[...]
            dsw = dy_ * (g * sg)
            outs, xhs = [], []
            for h in range(4):
                sl = slice(h * HGD, (h + 1) * HGD)
                oh = o[:, sl]
                r = lax.rsqrt(jnp.mean(oh * oh, axis=-1, keepdims=True) + EPS)
                xh = oh * r
                dxh = dsw[:, sl] * w[:, sl]
                outs.append(r * (dxh - xh * jnp.mean(dxh * xh, axis=-1, keepdims=True)))
                xhs.append(xh)
            xh = jnp.concatenate(xhs, axis=1)
            do_ref[...] = jnp.concatenate(outs, axis=1)
            dp_ref[...] = (dy_ * xh * w * (sg * (1.0 + g * (1.0 - sg)))).astype(BF16)
            dw_ref[...] += jnp.sum(dsw * xh, axis=0, keepdims=True)

    tok = pl.BlockSpec((TM, HGW), lambda i: (i, 0))
    lat = pl.BlockSpec((TM, HGW), lambda i: (jnp.maximum(i - 1, 0), 0))
    return _pcall(body, name="readout_bwd", grid=(tt // TM,),
                  in_specs=[tok, tok, pl.BlockSpec((TM, HGW), lambda i: (i, C_GHG)), _full((1, HGW)), lat, ANY],
                  out_specs=[pl.BlockSpec((TM, HGW), lambda i: (i, C_GHG)), lat, _full((1, HGW))],
                  out_shape=[jax.ShapeDtypeStruct(dp.shape, BF16), jax.ShapeDtypeStruct((s_len, HGW), F32),
                             jax.ShapeDtypeStruct((1, HGW), F32)],
                  aliases={5: 0}, carry=carry)(o0, o1, p, hw4, dy, dp)


def _rope_tables(s_len):
    t = np.arange(s_len)
    inv = ROPE_THETA ** (-np.arange(0, 32, 2, dtype=np.float64) / 32)
    def half(pos):
        ang = pos[:, None].astype(np.float64) * inv[None, :]
        return (np.concatenate([np.cos(ang), np.cos(ang)], 1), np.concatenate([-np.sin(ang), np.sin(ang)], 1))
    cr, sr = half(t // GRID_W)
    cc, sc = half(t % GRID_W)
    cos = np.concatenate([cr, cc, cr, cc], 1)
    sin = np.concatenate([sr, sc, sr, sc], 1)
    cos = np.concatenate([np.ones((L, 128)), cos], 0)
    sin = np.concatenate([np.zeros((L, 128)), sin], 0)
    return jnp.asarray(cos, F32), jnp.asarray(sin, F32)


def _blockdiag(n, w):
    i = np.arange(n)
    return jnp.asarray((i[:, None] // w == i[None, :] // w) / float(w), F32)


def _dup_matrix():
    m = np.zeros((128, 512), np.float32)
    for g in range(2):
        for j in range(4):
            for dd in range(HDIM):
                m[64 * g + dd, 256 * g + 64 * j + dd] = 1.0
    return m


def _head_mean(x, blockdiag):
    return _dot(x, blockdiag, prec=lax.Precision.HIGH)


def _rot(x):
    n = x.shape[1]
    lane = lax.broadcasted_iota(jnp.int32, x.shape, 1)
    return jnp.where((lane % 32) < 16, pltpu.roll(x, n - 16, 1), pltpu.roll(x, 16, 1))


def _qk_prep(p, cos, sin, qnw8, knw2, bd512, bd128, dup):
    tt = p.shape[0]

    def body(q_ref, kv_ref, cos_ref, sin_ref, qw_ref, kw_ref, b5_ref, b1_ref, dup_ref,
             qr_ref, k4_ref, v4_ref):
        cos_, sin_ = cos_ref[...], sin_ref[...]
        q = q_ref[...]
        qn = q * lax.rsqrt(_head_mean(q * q, b5_ref[...]) + EPS) * qw_ref[...]
        cos4 = jnp.concatenate([cos_] * 4, axis=1)
        sin4 = jnp.concatenate([sin_] * 4, axis=1)
        qr_ref[...] = ((qn * cos4 + _rot(qn) * sin4) * (HDIM ** -0.5)).astype(BF16)
        kv = kv_ref[...]
        k, v = kv[:, :128], kv[:, 128:]
        kn = k * lax.rsqrt(_head_mean(k * k, b1_ref[...]) + EPS) * kw_ref[...]
        kr = kn * cos_ + _rot(kn) * sin_
        k4_ref[...] = _bdot(kr, dup_ref[...]).astype(BF16)
        v4_ref[...] = _bdot(v, dup_ref[...]).astype(BF16)

    row = lambda w, cb: pl.BlockSpec((TM, w), lambda i: (i, cb))
    out = jax.ShapeDtypeStruct((tt, ATW), BF16)
    return _pcall(body, name="qk_prep", grid=(tt // TM,),
                  in_specs=[row(ATW, C_QRAW), row(256, C_KV), row(128, 0), row(128, 0),
                            _full((1, ATW)), _full((1, 128)), _full((ATW, ATW)), _full((128, 128)),
                            _full((128, ATW))],
                  out_specs=[row(ATW, 0)] * 3, out_shape=[out] * 3)(
                      p, p, cos, sin, qnw8, knw2, bd512, bd128, dup)


def _attn_masks(i, nb):
    r = lax.broadcasted_iota(jnp.int32, (4 * BLK, 3 * BLK + L), 0) % BLK
    c = lax.broadcasted_iota(jnp.int32, (4 * BLK, 3 * BLK + L), 1)
    kpos = (i - 1) * BLK + c
    loc = (jnp.abs(c - BLK - r) <= BLK) & (kpos >= 0) & (kpos < nb * BLK)
    return loc | (c >= 3 * BLK)


def _stack_mask():
    r = lax.broadcasted_iota(jnp.int32, (4 * BLK, 256), 0)
    lane = lax.broadcasted_iota(jnp.int32, (4 * BLK, 256), 1)
    return (r // BLK) == (lane // HDIM)


def _stack_heads(xg, fill=0.0):
    x4 = jnp.concatenate([xg] * 4, axis=0)
    return jnp.where(_stack_mask(), x4, jnp.full_like(x4, fill))


def _unstack_heads(x4):
    out = jnp.where(_lane_mask(0), x4[0:BLK], 0.0)
    for j in range(1, 4):
        out = out + jnp.where(_lane_mask(j), x4[j * BLK:(j + 1) * BLK], 0.0)
    return out


def _per_head_rows(vals):
    return jnp.concatenate([jnp.broadcast_to(v, (BLK, 1)) for v in vals], axis=0)


def _lane_mask(j):
    lane = lax.broadcasted_iota(jnp.int32, (1, 256), 1)
    return (lane // HDIM) == j


def _attn_specs(nb):
    blk = lambda off: pl.BlockSpec((BLK, ATW), lambda i: (jnp.clip(i + off, 0, nb - 1) + 2, 0))
    ctx = pl.BlockSpec((L, ATW), lambda i: (0, 0))
    return blk, ctx


def _attn_fwd(qr, k4, v4, sinks, carry=None):
    tt = qr.shape[0]
    s_len = tt - L
    nb = s_len // BLK

    def body(sk_ref, q_ref, kp, ko, kn, kc, vp, vo, vn, vc, y_ref, lse_ref):
        i = pl.program_id(0)
        valid = _attn_masks(i, nb)
        q = q_ref[...]
        ys, lses = [], []
        for g in range(2):
            gs = slice(256 * g, 256 * g + 256)
            kcat = jnp.concatenate([kp[:, gs], ko[:, gs], kn[:, gs], kc[:, gs]], axis=0)
            vcat = jnp.concatenate([vp[:, gs], vo[:, gs], vn[:, gs], vc[:, gs]], axis=0)
            sink4 = _per_head_rows([sk_ref[4 * g + j] for j in range(4)])
            q4 = _stack_heads(q[:, gs])
            o_parts, l_parts = [], []
            for hp in range(2):
                rows = slice(2 * BLK * hp, 2 * BLK * (hp + 1))
                sink = sink4[rows]
                s = jnp.where(valid[rows], _dot(q4[rows], kcat, NT), -1e30)
                m = jnp.maximum(jnp.max(s, axis=-1, keepdims=True), sink)
                e = jnp.exp(s - m)
                den = jnp.sum(e, axis=-1, keepdims=True) + jnp.exp(sink - m)
                o_parts.append(_bdot(e * (1.0 / den), vcat))
                l_parts.append(jnp.broadcast_to(m + jnp.log(den), (2 * BLK, 256)))
            ys.append(_unstack_heads(jnp.concatenate(o_parts, axis=0)))
            lses.append(_unstack_heads(jnp.concatenate(l_parts, axis=0)))
        y_ref[...] = jnp.concatenate(ys, axis=1).astype(BF16)
        lse_ref[...] = jnp.concatenate(lses, axis=1)

    blk, ctx = _attn_specs(nb)
    out = pl.BlockSpec((BLK, ATW), lambda i: (i, 0))
    return _pcall(body, name="attn_fwd", grid=(nb,),
                  in_specs=[pl.BlockSpec(memory_space=pltpu.SMEM), blk(0),
                            blk(-1), blk(0), blk(1), ctx, blk(-1), blk(0), blk(1), ctx],
                  out_specs=[out, out],
                  out_shape=[jax.ShapeDtypeStruct((s_len, ATW), BF16),
                             jax.ShapeDtypeStruct((s_len, ATW), F32)], carry=carry)(
                      sinks, qr, k4, k4, k4, k4, v4, v4, v4, v4)


def _attn_bwd(qr, k4, v4, sinks, y, lse, dy, carry=None):
    tt = qr.shape[0]
    s_len = tt - L
    nb = s_len // BLK

    def body(sk_ref, q_ref, kp, ko, kn, kc, vp, vo, vn, vc, y_ref, lse_ref, dy_ref,
             dq_ref, dkw_ref, dvw_ref, dkc_ref, dvc_ref, dsk_ref):
        i = pl.program_id(0)

        @pl.when(i == 0)
        def _():
            dkc_ref[...] = jnp.zeros_like(dkc_ref)
            dvc_ref[...] = jnp.zeros_like(dvc_ref)
            dsk_ref[...] = jnp.zeros_like(dsk_ref)

        valid = _attn_masks(i, nb)
        q = q_ref[...]
        dy_ = dy_ref[...]
        dly = dy_ * y_ref[...].astype(F32)
        lse_ = lse_ref[...]
        dqs = []
        for g in range(2):
            gs = slice(256 * g, 256 * g + 256)
            kcat = jnp.concatenate([kp[:, gs], ko[:, gs], kn[:, gs], kc[:, gs]], axis=0)
            vcat = jnp.concatenate([vp[:, gs], vo[:, gs], vn[:, gs], vc[:, gs]], axis=0)
            q4 = _stack_heads(q[:, gs])
            dy4 = _stack_heads(dy_[:, gs]).astype(BF16)
            lse4 = jnp.max(_stack_heads(lse_[:, gs], fill=-1e30), axis=-1, keepdims=True)
            delta = jnp.sum(_stack_heads(dly[:, gs]), axis=-1, keepdims=True)
            sink = _per_head_rows([sk_ref[4 * g + j] for j in range(4)])
            pr = jnp.where(valid, jnp.exp(_dot(q4, kcat, NT) - lse4), 0.0)
            dsb = (pr * (_dot(dy4, vcat, NT) - delta)).astype(BF16)
            dsink = jnp.exp(sink - lse4) * delta
            for j in range(4):
                dsk_ref[4 * g + j:4 * g + j + 1, :] += jnp.broadcast_to(
                    -jnp.sum(dsink[j * BLK:(j + 1) * BLK], axis=0, keepdims=True), (1, 128))
            dqs.append(_unstack_heads(_dot(dsb, kcat)))
            dkg = _dot(dsb, q4, TN)
            dvg = _dot(pr.astype(BF16), dy4, TN)
            dkw_ref[0, :, gs] = dkg[:3 * BLK]
            dvw_ref[0, :, gs] = dvg[:3 * BLK]
            dkc_ref[:, gs] += dkg[3 * BLK:]
            dvc_ref[:, gs] += dvg[3 * BLK:]
        dq_ref[...] = jnp.concatenate(dqs, axis=1)

    blk, ctx = _attn_specs(nb)
    out = pl.BlockSpec((BLK, ATW), lambda i: (i, 0))
    win = pl.BlockSpec((1, 3 * BLK, ATW), lambda i: (i, 0, 0))
    acc = _full((L, ATW))
    return _pcall(body, name="attn_bwd", grid=(nb,),
                  in_specs=[pl.BlockSpec(memory_space=pltpu.SMEM), blk(0),
                            blk(-1), blk(0), blk(1), ctx, blk(-1), blk(0), blk(1), ctx, out, out, out],
                  out_specs=[out, win, win, acc, acc, _full((8, 128))],
                  out_shape=[jax.ShapeDtypeStruct((s_len, ATW), F32),
                             jax.ShapeDtypeStruct((nb, 3 * BLK, ATW), F32),
                             jax.ShapeDtypeStruct((nb, 3 * BLK, ATW), F32),
                             jax.ShapeDtypeStruct((L, ATW), F32), jax.ShapeDtypeStruct((L, ATW), F32),
                             jax.ShapeDtypeStruct((8, 128), F32)], carry=carry)(
                      sinks, qr, k4, k4, k4, k4, v4, v4, v4, v4, y, lse, dy)


def _attn_post(p, cos, sin, qnw8, knw2, bd512, bd128, dupt, dq, dkw, dvw, dkc, dvc, dp, carry=None):
    tt = p.shape[0]
    s_len = tt - L
    nb = s_len // BLK
    nctx = L // BLK

    def body(q_ref, kv_ref, cos_ref, sin_ref, qw_ref, kw_ref, b5_ref, b1_ref, dupt_ref,
             dq_ref, kwp, kwo, kwn, vwp, vwo, vwn, dkc_ref, dvc_ref, _dp_in,
             dp_ref, dqw_ref, dkw_ref):
        t = pl.program_id(0)
        j = t - nctx

        @pl.when(t == 0)
        def _():
            dqw_ref[...] = jnp.zeros_like(dqw_ref)
            dkw_ref[...] = jnp.zeros_like(dkw_ref)

        is_lat = t >= nctx
        cos_, sin_ = cos_ref[...], sin_ref[...]
        has_p = is_lat & (j >= 1)
        has_n = is_lat & (j <= nb - 2)
        dk4 = (jnp.where(is_lat, kwo[0], dkc_ref[...]) + jnp.where(has_p, kwp[0], 0.0)
               + jnp.where(has_n, kwn[0], 0.0))
        dv4 = (jnp.where(is_lat, vwo[0], dvc_ref[...]) + jnp.where(has_p, vwp[0], 0.0)
               + jnp.where(has_n, vwn[0], 0.0))
        dkr = _dot(dk4, dupt_ref[...], prec=HI)
        dv = _dot(dv4, dupt_ref[...], prec=HI)
        kv = kv_ref[...]
        k = kv[:, :128]
        kw = kw_ref[...]
        rk = lax.rsqrt(_head_mean(k * k, b1_ref[...]) + EPS)
        xk = k * rk
        dkn = dkr * cos_ + _rot(dkr * sin_)
        dxk = dkn * kw
        dk = rk * (dxk - xk * _head_mean(dxk * xk, b1_ref[...]))
        dkw_ref[...] += jnp.sum(dkn * xk, axis=0, keepdims=True)
        q = q_ref[...]
        qw = qw_ref[...]
        rq = lax.rsqrt(_head_mean(q * q, b5_ref[...]) + EPS)
        xq = q * rq
        cos4 = jnp.concatenate([cos_] * 4, axis=1)
        sin4 = jnp.concatenate([sin_] * 4, axis=1)
        dqr = jnp.where(is_lat, dq_ref[...], 0.0) * (HDIM ** -0.5)
        dqn = dqr * cos4 + _rot(dqr * sin4)
        dxq = dqn * qw
        dqraw = rq * (dxq - xq * _head_mean(dxq * xq, b5_ref[...]))
        dqw_ref[...] += jnp.sum(dqn * xq, axis=0, keepdims=True)
        dp_ref[...] = jnp.concatenate([dqraw, dk, dv], axis=1).astype(BF16)

    row = lambda w, cb: pl.BlockSpec((BLK, w), lambda t: (t, cb))
    lat = pl.BlockSpec((BLK, ATW), lambda t: (jnp.maximum(t - nctx, 0), 0))

    def part(off):
        return pl.BlockSpec((1, BLK, ATW), lambda t: (jnp.clip(t - nctx + off, 0, nb - 1), 1 - off, 0))

    cacc = pl.BlockSpec((BLK, ATW), lambda t: (jnp.minimum(t, nctx - 1), 0))
    return _pcall(body, name="attn_post", grid=(tt // BLK,),
                  in_specs=[row(ATW, C_QRAW), row(256, C_KV), row(128, 0), row(128, 0),
                            _full((1, ATW)), _full((1, 128)), _full((ATW, ATW)), _full((128, 128)),
                            _full((ATW, 128)), lat, part(-1), part(0), part(1), part(-1), part(0), part(1),
                            cacc, cacc, ANY],
                  out_specs=[pl.BlockSpec((BLK, 768), lambda t: (t, C_QKV)), _full((1, ATW)), _full((1, 128))],
                  out_shape=[jax.ShapeDtypeStruct(dp.shape, BF16), jax.ShapeDtypeStruct((1, ATW), F32),
                             jax.ShapeDtypeStruct((1, 128), F32)],
                  aliases={18: 0}, carry=carry)(p, p, cos, sin, qnw8, knw2, bd512, bd128, dupt,
                                   dq, dkw, dkw, dkw, dvw, dvw, dvw, dkc, dvc, dp)


def _branch_merge(y_hg, y_at, bh4, ba4, p):
    s_len = y_hg.shape[0]

    def body(yh_ref, ya_ref, bh_ref, ba_ref, gh_ref, ga_ref, ah_ref, aa_ref, m_ref):
        yh, ya = yh_ref[...], ya_ref[...]
        ah = jnp.concatenate([_bdot(yh, bh_ref[j]) for j in range(4)], axis=1)
        aa = jnp.concatenate([_bdot(ya, ba_ref[j]) for j in range(4)], axis=1)
        ah_ref[...] = ah.astype(BF16)
        aa_ref[...] = aa.astype(BF16)
        m_ref[...] = (_sig(gh_ref[...]) * ah + _sig(ga_ref[...]) * aa).astype(BF16)

    row = pl.BlockSpec((TM, D), lambda i: (i, 0))
    y = pl.BlockSpec((TM, HGW), lambda i: (i, 0))
    f = jax.ShapeDtypeStruct((s_len, D), BF16)
    return _pcall(body, name="branch_merge", grid=(s_len // TM,),
                  in_specs=[y, y, _full(bh4.shape), _full(ba4.shape),
                            pl.BlockSpec((TM, D), lambda i: (i + 1, 2)), pl.BlockSpec((TM, D), lambda i: (i + 1, 3))],
                  out_specs=[row, row, row],
                  out_shape=[f, f, jax.ShapeDtypeStruct((s_len, D), BF16)])(y_hg, y_at, bh4, ba4, p, p)


def _branch_bwd(dmh, dma, bh4, ba4, y_hg, y_at):
    s_len = dmh.shape[0]
    nk = s_len // TS
    ns = D // 4

    def body(dh_ref, da_ref, bh_ref, ba_ref, yh_ref, ya_ref, dyh_ref, dya_ref, gh_ref, ga_ref, acc_h, acc_a):
        t = pl.program_id(0)

        @pl.when(t == 0)
        def _():
            acc_h[...] = jnp.zeros_like(acc_h)
            acc_a[...] = jnp.zeros_like(acc_a)

        for d_ref, w_ref, y_ref, dy_ref, acc in ((dh_ref, bh_ref, yh_ref, dyh_ref, acc_h),
                                                 (da_ref, ba_ref, ya_ref, dya_ref, acc_a)):
            y = y_ref[...]
            dy = jnp.zeros((TS, HGW), F32)
            for j in range(4):
                dj = d_ref[:, j * ns:(j + 1) * ns]
                dy = dy + _bdot(dj, w_ref[j], NT)
                acc[j] += _bdot(y, dj, TN)
            dy_ref[...] = dy

        @pl.when(t == nk - 1)
        def _():
            gh_ref[...] = acc_h[...].astype(BF16)
            ga_ref[...] = acc_a[...].astype(BF16)

    dm = pl.BlockSpec((TS, D), lambda t: (t, 0))
    y = pl.BlockSpec((TS, HGW), lambda t: (t, 0))
    w = _full(bh4.shape)
    fy = jax.ShapeDtypeStruct((s_len, HGW), F32)
    gw = jax.ShapeDtypeStruct(bh4.shape, BF16)
    return _pcall(body, name="branch_bwd", grid=(nk,), in_specs=[dm, dm, w, w, y, y],
                  out_specs=[y, y, w, w], out_shape=[fy, fy, gw, gw],
                  scratch=[pltpu.VMEM(bh4.shape, F32)] * 2)(dmh, dma, bh4, ba4, y_hg, y_at)


def _merge_bwd(dattn, w_o, mixed, ah, aa, p, carry=None):
    tt = p.shape[0]
    s_len = tt - L
    nt = tt // TM

    def body(da_ref, wo_ref, mx_ref, ah_ref, aa_ref, gh_ref, ga_ref, dp_ref, dmh_ref, dma_ref, go_ref, acc):
        i = pl.program_id(0)

        @pl.when(i == 0)
        def _():
            dp_ref[...] = jnp.zeros_like(dp_ref)
            acc[...] = jnp.zeros_like(acc)

        @pl.when(i >= 1)
        def _():
            da = da_ref[...]
            acc[...] += _bdot(mx_ref[...], da, TN)
            dm_ = _bdot(da, wo_ref[...], NT)
            sh, sa = _sig(gh_ref[...]), _sig(ga_ref[...])
            dp_ref[...] = jnp.concatenate([dm_ * ah_ref[...].astype(F32) * sh * (1.0 - sh),
                                           dm_ * aa_ref[...].astype(F32) * sa * (1.0 - sa)], axis=1).astype(BF16)
            dmh_ref[...] = (dm_ * sh).astype(BF16)
            dma_ref[...] = (dm_ * sa).astype(BF16)

        @pl.when(i == nt - 1)
        def _():
            go_ref[...] = acc[...].astype(BF16)

    lat = pl.BlockSpec((TM, D), lambda i: (jnp.maximum(i - 1, 0), 0))
    return _pcall(body, name="merge_bwd", grid=(nt,),
                  in_specs=[lat, _full((D, D)), lat, lat, lat, pl.BlockSpec((TM, D), lambda i: (i, 2)),
                            pl.BlockSpec((TM, D), lambda i: (i, 3))],
                  out_specs=[pl.BlockSpec((TM, 2 * D), lambda i: (i, C_GATES)), lat, lat, _full((D, D))],
                  out_shape=[jax.ShapeDtypeStruct((tt, NCOL), BF16), jax.ShapeDtypeStruct((s_len, D), BF16),
                             jax.ShapeDtypeStruct((s_len, D), BF16), jax.ShapeDtypeStruct((D, D), BF16)],
                  scratch=[pltpu.VMEM((D, D), F32)], carry=carry)(dattn, w_o, mixed, ah, aa, p, p)


def _local_step(x, ctx, tgt, mod, modc, nw1, nw2, lg, hw, qnw, knw, sinks,
                w_in, wts, dist=None):
    s_len = x.shape[0]
    tt = s_len + L
    ss1 = jnp.stack([modc, mod[0:2]])
    ss2 = mod[3:5][None]
    g1, g2 = mod[2:3], mod[5:6]
    hw4 = jnp.tile(hw, (1, 4))
    qnw8 = jnp.tile(qnw, (1, 8))
    knw2 = jnp.tile(knw, (1, 2))
    cos, sin = _rope_tables(s_len)
    bd512, bd128 = _blockdiag(ATW, HDIM), _blockdiag(128, HDIM)
    dupm = _dup_matrix()
    dup, dupt = jnp.asarray(dupm, BF16), jnp.asarray(dupm.T, F32)
    tmt = tt

    def four(b):
        return b.reshape(4, 2 * b.shape[1], b.shape[2])

    def halves(g):
        return g.reshape(4, 2, g.shape[1] // 2, g.shape[2])

    h = _mod1(ctx, x, nw1, ss1)
    if dist is None:
        bh4, ba4, w_o, g4, u4, dn4 = wts
        p = _mm_in(h, w_in, tmt)
        o0, st0 = _hgrn_fwd(p, lg, rev=False)
        o1, st1, y_hg = _hgrn_fwd(p, lg, rev=True, readout=(o0, hw4))
    else:
        core, chip = dist
        half = wts[3].shape[1] // 2
        p, (o8, g8a) = _mm_in(h, w_in, tmt, carry=_carry_gather([wts[2], wts[3]], rows=[None, (0, half)]))
        (o0, st0), (g8,) = _hgrn_fwd(p, lg, rev=False, carry=_carry_gather([g8a], rows=[(half, half)]))
        (o1, st1, y_hg), (dn8a, bh8, ba8) = _hgrn_fwd(
            p, lg, rev=True, readout=(o0, hw4),
            carry=_carry_gather([wts[5], wts[0], wts[1]], rows=[(0, half), None, None]))
        bh4, ba4, w_o, g4 = four(bh8), four(ba8), four(o8).reshape(D, D), four(g8)
    qr, k4, v4 = _qk_prep(p, cos, sin, qnw8, knw2, bd512, bd128, dup)
    if dist is None:
        y_at, lse = _attn_fwd(qr, k4, v4, sinks)
    else:
        (y_at, lse), (u8, dn8) = _attn_fwd(qr, k4, v4, sinks,
                                           carry=_carry_gather([wts[4], dn8a], rows=[None, (half, half)]))
        u4, dn4 = four(u8), four(dn8)
    ah, aa, mixed = _branch_merge(y_hg, y_at, bh4, ba4, p)
    ao, x1, h2 = _out_proj_mod2(mixed, w_o, x, g1, nw2, ss2)
    a4, b4, z4 = _ffn_up(h2, g4, u4)
    sq, dx2, dyb, dg2 = _ffn_down_loss(z4, dn4, x1, g2, tgt)

    da4, db4 = _ffn_dz(dyb, dn4, a4, b4)
    g_dn = _ffn_gdn(z4, dyb)
    if dist is None:
        dh2 = _ffn_dh2(da4, db4, g4, u4)
    else:
        dn_units = [halves(g_dn)]
        dh2, dn_recv = _ffn_dh2(da4, db4, g4, u4, carry=_carry_pairx(dn_units))
        dn_pairs = _rs_pair_add(dn_units, dn_recv, core)
    if dist is None:
        g_g, g_u = _ffn_ggu(h2, da4, db4)
    else:
        (g_g, g_u), c_dn = _ffn_ggu(h2, da4, db4, carry=_carry_chipx(dn_pairs))
        red_dn = _rs_chip_add(dn_pairs, c_dn, core, chip)
    dx1, dattn, dss2, dnw2, dg1 = _mod2_bwd(x1, dh2, dx2, ao, nw2, ss2, g1)
    if dist is None:
        dp, dmh, dma, g_o = _merge_bwd(dattn, w_o, mixed, ah, aa, p)
    else:
        gu_units = [halves(g_g), halves(g_u)]
        (dp, dmh, dma, g_o), gu_recv = _merge_bwd(dattn, w_o, mixed, ah, aa, p, carry=_carry_pairx(gu_units))
        ffn_pairs = list(dn_pairs) + list(_rs_pair_add(gu_units, gu_recv, core))
    dy_hg, dy_at, g_bh, g_ba = _branch_bwd(dmh, dma, bh4, ba4, y_hg, y_at)
    if dist is None:
        dp, do, dhw4 = _readout_bwd(o0, o1, p, hw4, dy_hg, dp)
        dq, dkw, dvw, dkc, dvc, dsk = _attn_bwd(qr, k4, v4, sinks, y_at, lse, dy_at)
        dp, dqnw8, dknw2 = _attn_post(p, cos, sin, qnw8, knw2, bd512, bd128, dupt, dq, dkw, dvw, dkc, dvc, dp)
    else:
        mix_units = [halves(g_bh), halves(g_ba), halves(g_o.reshape(4, D // 4, D))]
        (dp, do, dhw4), mix_recv = _readout_bwd(o0, o1, p, hw4, dy_hg, dp, carry=_carry_pairx(mix_units))
        mix_pairs = _rs_pair_add(mix_units, mix_recv, core)
        (dq, dkw, dvw, dkc, dvc, dsk), bwd = _attn_bwd(
            qr, k4, v4, sinks, y_at, lse, dy_at,
            carry=_carry_join(_carry_chipx(ffn_pairs[1:2]), _carry_sibx(red_dn)))
        red_g = _rs_chip_add(ffn_pairs[1:2], bwd[0:1], core, chip)
        (dp, dqnw8, dknw2), post = _attn_post(
            p, cos, sin, qnw8, knw2, bd512, bd128, dupt, dq, dkw, dvw, dkc, dvc, dp, carry=_carry_sibx(red_g))
    if dist is None:
        dp, dv0, dq0, dlg0 = _hgrn_bwd(p, lg, do, st0, dp, None, rev=False)
        dp, dlg1 = _hgrn_bwd(p, lg, do, st1, dp, (dv0, dq0), rev=True)
    else:
        (dp, dv0, dq0, dlg0), c_u = _hgrn_bwd(p, lg, do, st0, dp, None, rev=False,
                                              carry=_carry_chipx(ffn_pairs[2:3]))
        red_u = _rs_chip_add(ffn_pairs[2:3], c_u, core, chip)
        (dp, dlg1), last = _hgrn_bwd(p, lg, do, st1, dp, (dv0, dq0), rev=True,
                                     carry=_carry_join(_carry_chipx(mix_pairs), _carry_sibx(red_u)))
        mix_reds = _rs_chip_add(mix_pairs, last[0:3], core, chip)
        ffn_done = bwd[1:2] + post[0:1] + last[3:4]
    g_in = _mm_gin(dp, h, tmt)
    if dist is None:
        dh = _mm_dh(dp, w_in, tmt)
        gx, dss1, dnw1 = _mod1_bwd(ctx, x, dh, dx1, nw1, ss1)
        rs = None
    else:
        in_units = [halves(g_in.reshape(4, NCOL // 4, D))]
        in_pairs = _rs_pair_add(in_units, _pair_exchange(in_units), core)
        first_rows, rest_rows = _split_rows(in_pairs[0].shape[1], IN_ROWS_WITH_DH)
        small_rows, rest_rows = _split_rows(rest_rows[1], IN_ROWS_WITH_SMALL, first=rest_rows[0])
        dh, both = _mm_dh(dp, w_in, tmt, carry=_carry_join(_carry_chipx(in_pairs, rows=first_rows),
                                                           _carry_sibx(mix_reds)))
        in_part, mix_done = both[0:1], both[1:4]
        gx, dss1, dnw1 = _mod1_bwd(ctx, x, dh, dx1, nw1, ss1)
        rs = dict(ffn_done=ffn_done, mix_done=mix_done, in_pairs=in_pairs, in_part=in_part, small_rows=small_rows,
                  rest_rows=rest_rows)

    dmod = jnp.concatenate([dss1[1], dg1, dss2, dg2], axis=0)
    dmodc = dss1[0]
    raw = (dss1, dg1, dss2, dg2, dnw1, dnw2, dhw4, dqnw8, dknw2, dsk, dlg0, dlg1)
    small = dict(raw=raw, dmod=dmod, dmodc=dmodc, dnw1=dnw1, dnw2=dnw2,
                 dhw=dhw4.reshape(4, HGD).sum(0, keepdims=True),
                 dqnw=dqnw8.reshape(8, HDIM).sum(0, keepdims=True),
                 dknw=dknw2.reshape(2, HDIM).sum(0, keepdims=True),
                 dsinks=dsk[:, 0], dlg=jnp.concatenate([dlg0, dlg1], axis=0))
    big = dict(w_in=g_in, w_bh=g_bh, w_ba=g_ba, w_o=g_o, w_g=g_g, w_u=g_u, w_dn=g_dn)
    return sq, gx, big, small, rs


def _place():
    x, y, c = lax.axis_index("x"), lax.axis_index("y"), lax.axis_index("c")
    return x, y, c


def _gather_blocks(x_refs, out_refs, send_sems, recv_sems, local_sems):
    n = len(out_refs)
    x, y, c = _place()
    me, sibling = (x, y, c), (x, y, 1 - c)
    chips = [(1 - x, y), (x, 1 - y), (1 - x, 1 - y)]

    def slot(u, px, py, pc):
        return out_refs[u].at[4 * px + 2 * py + pc]

    def copy(u, k, block, to, src=None):
        return pltpu.make_async_remote_copy(
            src_ref=slot(u, *block) if src is None else src, dst_ref=slot(u, *block),
            send_sem=send_sems.at[u, k], recv_sem=recv_sems.at[u, k], device_id=to, device_id_type=MESH)

    mines = [pltpu.make_async_copy(x_refs[u], slot(u, *me), local_sems.at[u]) for u in range(n)]
    for cp in mines:
        cp.start()
    first = []
    for u in range(n):
        first.append(copy(u, 0, me, sibling, src=x_refs[u]))
        first += [copy(u, 1 + j, me, (*chip, c), src=x_refs[u]) for j, chip in enumerate(chips)]
    for cp in first:
        cp.start()
    passed = []
    for j, chip in enumerate(chips):
        for u in range(n):
            copy(u, 1 + j, (*chip, c), me).wait_recv()
            fwd = copy(u, 4 + j, (*chip, c), sibling)
            fwd.start()
            passed.append(fwd)
    for u in range(n):
        copy(u, 0, sibling, me).wait_recv()
    for j, chip in enumerate(chips):
        for u in range(n):
            copy(u, 4 + j, (*chip, 1 - c), me).wait_recv()
    for cp in first + passed:
        cp.wait_send()
    for cp in mines:
        cp.wait()


def _gather_sems(n):
    return [pltpu.SemaphoreType.DMA((n, 7)), pltpu.SemaphoreType.DMA((n, 7)), pltpu.SemaphoreType.DMA((n,))]


def _cast_place(ws, c, dev):
    n = len(ws)

    def body(s_ref, *refs):
        for u in range(n):
            refs[n + u][0] = refs[u][...].astype(BF16)

    in_specs, out_specs, out_shape = [], [], []
    for w in ws:
        q, cols = w.shape[0] // 4, w.shape[1]
        in_specs.append(pl.BlockSpec((q, cols), lambda i, s: (2 * s[0] + i, 0)))
        out_specs.append(pl.BlockSpec((1, q, cols), lambda i, s: (s[1], i, 0)))
        out_shape.append(jax.ShapeDtypeStruct((8, 2 * q, cols), BF16))
    return pl.pallas_call(
        body, name="cast_place",
        grid_spec=pltpu.PrefetchScalarGridSpec(num_scalar_prefetch=1, grid=(2,), in_specs=in_specs,
                                               out_specs=out_specs),
        out_shape=_out_hbm(out_shape),
        compiler_params=pltpu.CompilerParams(vmem_limit_bytes=48 << 20))(jnp.stack([c, dev]), *_in_hbm(ws))


def _gather_phases(out_refs, send_sems, recv_sems, rows=None):
    n = len(out_refs)
    x, y, c = _place()
    me, sibling = (x, y, c), (x, y, 1 - c)
    chips = [(1 - x, y), (x, 1 - y), (1 - x, 1 - y)]

    def copy(u, k, block, to):
        px, py, pc = block
        ref = out_refs[u].at[4 * px + 2 * py + pc]
        if rows is not None and rows[u] is not None:
            ref = ref.at[pl.ds(rows[u][0], rows[u][1])]
        return pltpu.make_async_remote_copy(src_ref=ref, dst_ref=ref, send_sem=send_sems.at[u, k],
                                            recv_sem=recv_sems.at[u, k], device_id=to, device_id_type=MESH)

    def start():
        for u in range(n):
            copy(u, 0, me, sibling).start()
            for j, chip in enumerate(chips):
                copy(u, 1 + j, me, (*chip, c)).start()

    def mid():
        for j, chip in enumerate(chips):
            for u in range(n):
                copy(u, 1 + j, (*chip, c), me).wait_recv()
                copy(u, 4 + j, (*chip, c), sibling).start()

    def end():
        for u in range(n):
            copy(u, 0, sibling, me).wait_recv()
        for j, chip in enumerate(chips):
            for u in range(n):
                copy(u, 4 + j, (*chip, 1 - c), me).wait_recv()
        for u in range(n):
            copy(u, 0, me, sibling).wait_send()
            for j, chip in enumerate(chips):
                copy(u, 1 + j, me, (*chip, c)).wait_send()
                copy(u, 4 + j, (*chip, c), sibling).wait_send()

    return start, mid, end


def _carry_gather(bufs, rows=None):
    n = len(bufs)
    return _Carry(bufs, [jax.ShapeDtypeStruct(b.shape, b.dtype) for b in bufs], {u: u for u in range(n)},
                  [pltpu.SemaphoreType.DMA((n, 7)), pltpu.SemaphoreType.DMA((n, 7))],
                  lambda ins, outs, sems: _gather_phases(outs, *sems, rows=rows), "both")


def _ag_small(raw, sq, pairs, rows, into):
    def body(dss1, dg1, dss2, dg2, dnw1, dnw2, dhw4, dqnw8, dknw2, dsk, dlg0, dlg1, sq_ref, p_ref, _into,
             out_ref, tot_ref, r_ref, blk, send_sems, recv_sems, chip_send, chip_recv):
        _peer_barrier("both")
        x, y, c = _place()
        start, mid, end = _gather_phases([out_ref], send_sems, recv_sems)
        chip_start, _, chip_end = _chipx_phases([p_ref], [r_ref], chip_send, chip_recv, rows=rows)
        blk[...] = jnp.zeros_like(blk)
        blk[0:2, :] = dss1[1]
        blk[2:3, :] = dg1[...]
        blk[3:5, :] = dss2[...]
        blk[5:6, :] = dg2[...]
        blk[6:8, :] = dss1[0]
        blk[8:9, :] = dnw1[...]
        blk[9:10, :] = dnw2[...]
        blk[10:11, 0:HGW] = dhw4[...]
        blk[10:11, HGW:D] = dqnw8[...]
        blk[11:12, 0:128] = dknw2[...]
        blk[12:14, 0:HGW] = dlg0[0]
        blk[14:16, 0:HGW] = dlg1[0]
        blk[16:24, 0:128] = dsk[...]
        blk[24:25, :] = sq_ref[...]
        out_ref[4 * x + 2 * y + c] = blk[...]
        start()
        chip_start()
        mid()
        end()
        acc = out_ref[0]
        for i in range(1, 8):
            acc = acc + out_ref[i]
        tot_ref[...] = acc
        chip_end()

    vm = pl.BlockSpec(memory_space=pltpu.VMEM)
    g2, tot, part = pl.pallas_call(
        body, name="ag_small",
        out_shape=[jax.ShapeDtypeStruct((8, 32, D), F32), jax.ShapeDtypeStruct((32, D), F32),
                   jax.ShapeDtypeStruct(into.shape, into.dtype)],
        in_specs=[vm] * 13 + [ANY, ANY], out_specs=[vm, vm, ANY], input_output_aliases={14: 2},
        scratch_shapes=[pltpu.VMEM((32, D), F32)] + [pltpu.SemaphoreType.DMA((1, 7))] * 2
        + [pltpu.SemaphoreType.DMA((1, 3))] * 2,
        compiler_params=pltpu.CompilerParams(collective_id=BARRIER_IDS["both"]))(*raw, sq, *_in_hbm([pairs[0], into]))
    return (g2, tot), part


def _pairx_shapes(units):
    return [jax.ShapeDtypeStruct((4,) + g.shape[2:], g.dtype) for g in units]


def _pairx_phases(g_refs, r_refs, send_sems, recv_sems):
    n = len(g_refs)
    x, y, c = _place()
    cps = [pltpu.make_async_remote_copy(
        src_ref=g_refs[u].at[j, 1 - c], dst_ref=r_refs[u].at[j], send_sem=send_sems.at[u, j],
        recv_sem=recv_sems.at[u, j], device_id=(x, y, 1 - c), device_id_type=MESH)
        for u in range(n) for j in range(4)]

    def start():
        for cp in cps:
            cp.start()

    def end():
        for cp in cps:
            cp.wait()

    return start, None, end


def _carry_pairx(units):
    n = len(units)
    return _Carry(units, _pairx_shapes(units), {},
                  [pltpu.SemaphoreType.DMA((n, 4)), pltpu.SemaphoreType.DMA((n, 4))],
                  lambda ins, outs, sems: _pairx_phases(ins, outs, *sems), "sib")


def _pair_exchange(units):
    n = len(units)

    def body(*refs):
        _peer_barrier("sib")
        start, _, end = _pairx_phases(refs[:n], refs[n:2 * n], *refs[2 * n:])
        start()
        end()

    return pl.pallas_call(
        body, name="pair_exchange", out_shape=_pairx_shapes(units), in_specs=[ANY] * n, out_specs=[ANY] * n,
        scratch_shapes=[pltpu.SemaphoreType.DMA((n, 4))] * 2,
        compiler_params=pltpu.CompilerParams(collective_id=BARRIER_IDS["sib"]))(*_in_hbm(units))


IN_ROWS_WITH_DH = 0.6


IN_ROWS_WITH_SMALL = 0.45


def _split_rows(h, share, first=0):
    k = int(h * share) // BF16_SUBLANES * BF16_SUBLANES
    return (first, k), (first + k, h - k)


def _rs_pair_add(units, recvs, c):
    n = len(units)

    def body(c_ref, *refs):
        for u in range(n):
            refs[2 * n + u][...] = (refs[u][0].astype(F32) + refs[n + u][...].astype(F32)).astype(BF16)

    in_specs, out_specs, out_shape = [], [], []
    for g in units:
        h, w = g.shape[2:]
        in_specs.append(pl.BlockSpec((1, 1, h, w), lambda j, cr: (j, cr[0], 0, 0)))
    for g in units:
        h, w = g.shape[2:]
        in_specs.append(pl.BlockSpec((1, h, w), lambda j, cr: (j, 0, 0)))
        out_specs.append(pl.BlockSpec((1, h, w), lambda j, cr: (j, 0, 0)))
        out_shape.append(jax.ShapeDtypeStruct((4, h, w), BF16))
    return pl.pallas_call(
        body, name="rs_pair_add",
        grid_spec=pltpu.PrefetchScalarGridSpec(num_scalar_prefetch=1, grid=(4,), in_specs=in_specs,
                                               out_specs=out_specs),
        out_shape=_out_hbm(out_shape),
        compiler_params=pltpu.CompilerParams(vmem_limit_bytes=48 << 20))(
            c.reshape(1), *_in_hbm(list(units) + list(recvs)))


def _chipx_phases(p_refs, r_refs, send_sems, recv_sems, rows=None):
    n = len(p_refs)
    x, y, c = _place()
    k = 2 * x + y

    def part(ref):
        return ref if rows is None else ref.at[pl.ds(rows[0], rows[1])]

    sends = []
    for d in range(1, 4):
        j = (k + d) % 4
        for u in range(n):
            sends.append(pltpu.make_async_remote_copy(
                src_ref=part(p_refs[u].at[j]), dst_ref=part(r_refs[u].at[k]), send_sem=send_sems.at[u, d - 1],
                recv_sem=recv_sems.at[u, d - 1], device_id=(j // 2, j % 2, c), device_id_type=MESH))

    def start():
        for cp in sends:
            cp.start()

    def end():
        for d in range(1, 4):
            src = (k + 4 - d) % 4
            for u in range(n):
                pltpu.make_async_remote_copy(
                    src_ref=part(p_refs[u].at[src]), dst_ref=part(r_refs[u].at[src]),
                    send_sem=send_sems.at[u, d - 1], recv_sem=recv_sems.at[u, d - 1], device_id=(x, y, c),
                    device_id_type=MESH).wait_recv()
        for cp in sends:
            cp.wait_send()

    return start, None, end


def _carry_chipx(pairs, rows=None, into=None):
    n = len(pairs)
    sems = [pltpu.SemaphoreType.DMA((n, 3)), pltpu.SemaphoreType.DMA((n, 3))]
    shapes = [jax.ShapeDtypeStruct(p.shape, p.dtype) for p in pairs]
    if into is None:
        return _Carry(pairs, shapes, {}, sems, lambda ins, outs, s: _chipx_phases(ins, outs, *s, rows=rows),
                      "chips")
    return _Carry(list(pairs) + list(into), shapes, {n + u: u for u in range(n)}, sems,
                  lambda ins, outs, s: _chipx_phases(ins[:n], outs, *s, rows=rows), "chips")


def _rs_chip_add(pairs, contribs, c, chip):
    n = len(pairs)

    def body(s_ref, *refs):
        for u in range(n):
            a, b, c_, d = refs[4 * u:4 * u + 4]
            refs[4 * n + u][0] = ((a[0].astype(F32) + b[0].astype(F32)) + c_[0].astype(F32)) + d[0].astype(F32)

    in_specs, out_specs, out_shape, args = [], [], [], []
    for p, r in zip(pairs, contribs):
        h, w = p.shape[1] // 2, p.shape[2]
        in_specs += [pl.BlockSpec((1, h, w), functools.partial(lambda d, i, s: ((s[1] + d) % 4, i, 0), d))
                     for d in range(4)]
        args += [p, r, r, r]
        out_specs.append(pl.BlockSpec((1, h, w), lambda i, s: (s[0], i, 0)))
        out_shape.append(jax.ShapeDtypeStruct((2, 2 * h, w), F32))
    return pl.pallas_call(
        body, name="rs_chip_add",
        grid_spec=pltpu.PrefetchScalarGridSpec(num_scalar_prefetch=1, grid=(2,), in_specs=in_specs,
                                               out_specs=out_specs),
        out_shape=_out_hbm(out_shape),
        compiler_params=pltpu.CompilerParams(vmem_limit_bytes=48 << 20))(jnp.stack([c, chip]), *_in_hbm(args))


def _rs_sibling_gather(reds, blks):
    n, k = len(reds), len(blks)
    vm = pl.BlockSpec(memory_space=pltpu.VMEM)

    def body(*refs):
        blk_in, red_out, blk_out = refs[n:n + k], refs[n + k:2 * n + k], refs[2 * n + k:2 * (n + k)]
        sems = refs[2 * (n + k):]
        _peer_barrier("both")
        start, _, end = _sibx_phases(red_out, *sems[:2])
        start()
        _gather_blocks(blk_in, blk_out, *sems[2:])
        end()

    res = pl.pallas_call(
        body, name="rs_sibling_gather",
        out_shape=[jax.ShapeDtypeStruct(r.shape, r.dtype) for r in reds]
        + [jax.ShapeDtypeStruct((8,) + b.shape, b.dtype) for b in blks],
        in_specs=[ANY] * n + [vm] * k, out_specs=[ANY] * n + [vm] * k,
        input_output_aliases={u: u for u in range(n)},
        scratch_shapes=[pltpu.SemaphoreType.DMA((n,))] * 2 + _gather_sems(k),
        compiler_params=pltpu.CompilerParams(collective_id=BARRIER_IDS["both"]))(*_in_hbm(reds), *blks)
    return res[:n], res[n:]


def _sibx_phases(o_refs, send_sems, recv_sems):
    n = len(o_refs)
    x, y, c = _place()
    cps = [pltpu.make_async_remote_copy(
        src_ref=o_refs[u].at[c], dst_ref=o_refs[u].at[c], send_sem=send_sems.at[u], recv_sem=recv_sems.at[u],
        device_id=(x, y, 1 - c), device_id_type=MESH) for u in range(n)]

    def start():
        for cp in cps:
            cp.start()

    def end():
        for u in range(n):
            cps[u].wait_send()
            pltpu.make_async_remote_copy(
                src_ref=o_refs[u].at[1 - c], dst_ref=o_refs[u].at[1 - c], send_sem=send_sems.at[u],
                recv_sem=recv_sems.at[u], device_id=(x, y, 1 - c), device_id_type=MESH).wait_recv()

    return start, None, end


def _carry_sibx(reds):
    n = len(reds)
    return _Carry(reds, [jax.ShapeDtypeStruct(r.shape, r.dtype) for r in reds], {u: u for u in range(n)},
                  [pltpu.SemaphoreType.DMA((n,))] * 2, lambda ins, outs, sems: _sibx_phases(outs, *sems), "sib")


def _prologue(blk, c_ctx, w, b, in8):
    n = w.shape[1]

    def body(blk_ref, cctx_ref, w_ref, b_ref, _in_in, g0_ref, c16_ref, g1_ref, in_ref, s1, r1, l1, s2, r2, s3, r3):
        _peer_barrier("both")
        x, y, c = _place()
        start, mid, end = _gather_phases([in_ref], s3, r3)
        start_mod, mid_mod, end_mod = _gather_phases([g1_ref], s2, r2)
        _gather_blocks([blk_ref], [g0_ref], s1, r1, l1)
        start()
        c16 = jnp.concatenate([g0_ref[i, 0:1, :] for i in range(8)] + [cctx_ref[...], jnp.zeros((7, D), F32)],
                              axis=0)
        c16_ref[...] = c16
        g1_ref[4 * x + 2 * y + c] = _dot(c16 * _sig(c16), w_ref[...], prec=HI) + b_ref[...]
        start_mod()
        mid()
        mid_mod()
        end()
        end_mod()

    vm = pl.BlockSpec(memory_space=pltpu.VMEM)
    return pl.pallas_call(
        body, name="prologue",
        out_shape=[jax.ShapeDtypeStruct((8, 8, D), F32), jax.ShapeDtypeStruct((16, D), F32),
                   jax.ShapeDtypeStruct((8, 16, n), F32), jax.ShapeDtypeStruct(in8.shape, in8.dtype)],
        in_specs=[vm, vm, vm, vm, ANY], out_specs=[vm, vm, vm, ANY], input_output_aliases={4: 3},
        scratch_shapes=_gather_sems(1) + [pltpu.SemaphoreType.DMA((1, 7))] * 4,
        compiler_params=pltpu.CompilerParams(vmem_limit_bytes=48 << 20,
                                             collective_id=BARRIER_IDS["both"]))(blk, c_ctx, w, b, in8)


def _ada_bwd(c16, dmod16, w, carry=None):
    n = w.shape[1]
    tn = 512

    def body(c_ref, d_ref, w_ref, gw_ref, gc_ref):
        j = pl.program_id(0)

        @pl.when(j == 0)
        def _():
            gc_ref[...] = jnp.zeros_like(gc_ref)

        cc = c_ref[...]
        dm = d_ref[...]
        gw_ref[...] = _dot(cc * _sig(cc), dm, TN, prec=HI)
        gc_ref[...] += _dot(dm, w_ref[...], NT, prec=HI)

    return _pcall(body, name="ada_bwd", grid=(n // tn,),
                  in_specs=[_full((16, D)), pl.BlockSpec((16, tn), lambda j: (0, j)),
                            pl.BlockSpec((D, tn), lambda j: (0, j))],
                  out_specs=[pl.BlockSpec((D, tn), lambda j: (0, j)), _full((16, D))],
                  out_shape=[jax.ShapeDtypeStruct((D, n), F32),
                             jax.ShapeDtypeStruct((16, D), F32)], carry=carry)(c16, dmod16, w)


def _adam_math(w, g, m, v):
    c1 = 1.0 - ADAM_B1 ** ADAM_STEP
    c2 = 1.0 - ADAM_B2 ** ADAM_STEP
    nm = ADAM_B1 * m + (1.0 - ADAM_B1) * g
    nv = ADAM_B2 * v + (1.0 - ADAM_B2) * (g * g)
    return -ADAM_LR * ((nm / c1) / (jnp.sqrt(nv / c2) + ADAM_EPS) + ADAM_WD * w), nm, nv


def _adamw_small(ws, gs, ms, vs):
    n = len(ws)

    def body(*refs):
        for u in range(n):
            d_, nm, nv = _adam_math(refs[u][...], refs[n + u][...], refs[2 * n + u][...], refs[3 * n + u][...])
            refs[4 * n + u][...] = d_
            refs[5 * n + u][...] = nm
            refs[6 * n + u][...] = nv

    specs = [_full(w.shape) for w in ws]
    shapes = [jax.ShapeDtypeStruct(w.shape, F32) for w in ws]
    out = _pcall(body, name="adamw_small", grid=(1,), in_specs=specs * 4, out_specs=specs * 3,
                 out_shape=shapes * 3)(*ws, *gs, *ms, *vs)
    return out[:n], out[n:2 * n], out[2 * n:]


def _cctx_grad(parts, c_ctx):
    def body(p_ref, c_ref, o_ref):
        acc = p_ref[0:1, :]
        for k in range(1, 4):
            acc = acc + p_ref[k:k + 1, :]
        cc = c_ref[...]
        s = _sig(cc)
        o_ref[...] = acc * (s * (1.0 + cc * (1.0 - s)))

    return _pcall(body, name="cctx_grad", grid=(1,), in_specs=[_full(parts.shape), _full((1, D))],
                  out_specs=_full((1, D)), out_shape=jax.ShapeDtypeStruct((1, D), F32))(parts, c_ctx)


ADAM_STEPS = 8


def _adamw_multi(ws, gs, ms, vs, *, name):
    n = len(ws)

    def body(*refs):
        for u in range(n):
            g = refs[n + u][...]
            refs[4 * n + u][...] = g
            refs[5 * n + u][...], refs[6 * n + u][...], refs[7 * n + u][...] = _adam_math(
                refs[u][...], g, refs[2 * n + u][...], refs[3 * n + u][...])

    specs = [pl.BlockSpec((w.shape[0] // ADAM_STEPS, w.shape[1]), lambda i: (i, 0)) for w in ws]
    shapes = [jax.ShapeDtypeStruct(w.shape, F32) for w in ws]
    out = _pcall(body, name=name, grid=(ADAM_STEPS,), in_specs=specs * 4, out_specs=specs * 4,
                 out_shape=shapes * 4)(*ws, *gs, *ms, *vs)
    return out[:n], out[n:2 * n], out[2 * n:3 * n], out[3 * n:]


def kernel(x, c, ctx, c_ctx, w_ada, b_ada, norm_mix_w, norm_ffn_w, w_in, hgrn_lb_logits, hgrn_norm_w, q_norm_w, k_norm_w, attn_sinks, w_branch_hgrn, w_branch_attn, w_out, w_ffn_gate, w_ffn_up, w_ffn_down, loss_target, m_c_ctx, m_w_ada, m_b_ada, m_norm_mix_w, m_norm_ffn_w, m_w_in, m_hgrn_lb_logits, m_hgrn_norm_w, m_q_norm_w, m_k_norm_w, m_attn_sinks, m_w_branch_hgrn, m_w_branch_attn, m_w_out, m_w_ffn_gate, m_w_ffn_up, m_w_ffn_down, v_c_ctx, v_w_ada, v_b_ada, v_norm_mix_w, v_norm_ffn_w, v_w_in, v_hgrn_lb_logits, v_hgrn_norm_w, v_q_norm_w, v_k_norm_w, v_attn_sinks, v_w_branch_hgrn, v_w_branch_attn, v_w_out, v_w_ffn_gate, v_w_ffn_up, v_w_ffn_down):
    xi, yi, ci = _place()
    chip = 2 * xi + yi
    dev = 2 * chip + ci
    s_len = x.shape[1]

    shards = [w_in[0].T, w_branch_hgrn[0], w_branch_attn[0], w_out[0], w_ffn_gate[0].T, w_ffn_up[0].T,
              w_ffn_down[0]]
    bufs = _cast_place(shards, ci, dev)

    lbrow = jnp.pad(hgrn_lb_logits.reshape(1, 512), ((0, 0), (0, D - 512)))
    blk = jnp.concatenate([c, lbrow, jnp.zeros((6, D), F32)], axis=0)
    nada = w_ada.shape[2]
    b_sh = lax.dynamic_slice(b_ada, (0, chip * nada), (1, nada))
    g0, c16, g1, in8 = _prologue(blk, c_ctx[None], w_ada[0], b_sh, bufs[0])
    lg = g0[0::2, 1, :512].reshape(4, 2, 2, 128).transpose(1, 2, 0, 3).reshape(2, 2, HGW)
    modall = g1[0::2].transpose(1, 0, 2).reshape(16, 4 * nada)
    mod = lax.dynamic_slice(modall, (dev, 0), (1, 6 * D)).reshape(6, D)
    modc = modall[8].reshape(6, D)[:2]

    sq, gx, _, small, rs = _local_step(
        x[0], ctx[0], loss_target[0], mod, modc, norm_mix_w, norm_ffn_w, lg, hgrn_norm_w, q_norm_w,
        k_norm_w, attn_sinks[0], in8.reshape(NCOL, D), bufs[1:], dist=(ci, chip))

    def whole(r):
        return r.reshape(2 * r.shape[1], r.shape[2])

    g_dn, g_g, g_u = [whole(r) for r in rs["ffn_done"]]
    g_bh, g_ba, g_o = [whole(r) for r in rs["mix_done"]]
    in_pairs = rs["in_pairs"]

    (g2, tot), in_part = _ag_small(small["raw"], sq, in_pairs, rs["small_rows"], rs["in_part"][0])
    loss = 0.5 * jnp.sum(tot[24]) / D
    dmodc_tot = jnp.pad(tot[6:8].reshape(1, 2 * D), ((0, 0), (0, 4 * D)))
    g_b_ada = tot[0:6].reshape(1, 6 * D) + dmodc_tot
    dmod16 = jnp.concatenate([g2[:, 0:6].reshape(8, 6 * D), dmodc_tot, jnp.zeros((7, 6 * D), F32)], axis=0)
    (g_w_ada, gc_part), in_contribs = _ada_bwd(
        c16, lax.dynamic_slice(dmod16, (0, chip * nada), (16, nada)), w_ada[0],
        carry=_carry_chipx(in_pairs, rows=rs["rest_rows"], into=[in_part]))
    in_reds, (g3,) = _rs_sibling_gather(_rs_chip_add(in_pairs, in_contribs, ci, chip), [gc_part[8:16]])
    g_in = whole(in_reds[0])
    g_c_ctx = _cctx_grad(g3[0::2, 0], c_ctx[None])[0]
    g_nw1 = tot[8:9]
    g_nw2 = tot[9:10]
    g_hw = tot[10, :HGW].reshape(4, HGD).sum(0, keepdims=True)
    g_qnw = tot[10, HGW:].reshape(8, HDIM).sum(0, keepdims=True)
    g_knw = tot[11, :128].reshape(2, HDIM).sum(0, keepdims=True)
    g_sinks = tot[16:24, 0][None]
    g_lg = lax.dynamic_slice(tot[12:16, :HGW].reshape(2, 2, HGW), (0, 0, chip * 128), (2, 2, 128))

    names = ["c_ctx", "w_ada", "b_ada", "norm_mix_w", "norm_ffn_w", "w_in", "hgrn_lb_logits", "hgrn_norm_w",
             "q_norm_w", "k_norm_w", "attn_sinks", "w_branch_hgrn", "w_branch_attn", "w_out", "w_ffn_gate",
             "w_ffn_up", "w_ffn_down"]
    ws = dict(zip(names, [c_ctx, w_ada, b_ada, norm_mix_w, norm_ffn_w, w_in, hgrn_lb_logits, hgrn_norm_w,
                          q_norm_w, k_norm_w, attn_sinks, w_branch_hgrn, w_branch_attn, w_out, w_ffn_gate,
                          w_ffn_up, w_ffn_down]))
    ms = dict(zip(names, [m_c_ctx, m_w_ada, m_b_ada, m_norm_mix_w, m_norm_ffn_w, m_w_in, m_hgrn_lb_logits,
                          m_hgrn_norm_w, m_q_norm_w, m_k_norm_w, m_attn_sinks, m_w_branch_hgrn,
                          m_w_branch_attn, m_w_out, m_w_ffn_gate, m_w_ffn_up, m_w_ffn_down]))
    vs = dict(zip(names, [v_c_ctx, v_w_ada, v_b_ada, v_norm_mix_w, v_norm_ffn_w, v_w_in, v_hgrn_lb_logits,
                          v_hgrn_norm_w, v_q_norm_w, v_k_norm_w, v_attn_sinks, v_w_branch_hgrn,
                          v_w_branch_attn, v_w_out, v_w_ffn_gate, v_w_ffn_up, v_w_ffn_down]))
    transposed = ("w_in", "w_ffn_gate", "w_ffn_up")

    def view(a, n):
        return a[0].T if n in transposed else a[0]

    def unview(a, n):
        return a.T[None] if n in transposed else a[None]

    delta, new_m, new_v, grads = {}, {}, {}, {}

    def big_adamw(group, gs, name):
        g_, d_, m_, v_ = _adamw_multi([view(ws[n], n) for n in group], gs, [view(ms[n], n) for n in group],
                                      [view(vs[n], n) for n in group], name=name)
        for i, n in enumerate(group):
            grads[n], delta[n], new_m[n], new_v[n] = (unview(g_[i], n), unview(d_[i], n), unview(m_[i], n),
                                                      unview(v_[i], n))

    big_adamw(["w_ffn_down", "w_ffn_gate", "w_ffn_up", "w_out", "w_branch_hgrn", "w_branch_attn"],
              [g_dn, g_g, g_u, g_o, g_bh, g_ba], "adamw_first")
    big_adamw(["w_in", "w_ada"], [g_in, g_w_ada], "adamw_second")
    grads.update(c_ctx=g_c_ctx, b_ada=g_b_ada, norm_mix_w=g_nw1, norm_ffn_w=g_nw2, hgrn_lb_logits=g_lg,
                 hgrn_norm_w=g_hw, q_norm_w=g_qnw, k_norm_w=g_knw, attn_sinks=g_sinks)
    small_names = [n for n in names if n not in delta]

    def two_d(a):
        return a.reshape(1, -1) if a.ndim == 1 else a

    sd, sm_, sv = _adamw_small(*[[two_d(d[n]) for n in small_names] for d in (ws, grads, ms, vs)])
    for i, n in enumerate(small_names):
        for dst, src in ((delta, sd), (new_m, sm_), (new_v, sv)):
            dst[n] = src[i].reshape(ws[n].shape)
    return (loss, gx[None], *[grads[n] for n in names], *[delta[n] for n in names],
            *[new_m[n] for n in names], *[new_v[n] for n in names])
```

```python
import functools

import numpy as np
import jax
import jax.numpy as jnp
from jax import lax
from jax.experimental import pallas as pl
from jax.experimental.pallas import tpu as pltpu

F32 = jnp.float32
BF16 = jnp.bfloat16
HI = lax.Precision.HIGHEST
MESH = pl.DeviceIdType.MESH

D = 1024
L = 256
TM = 256
HGW = 512
HGD = 128
CH = 32
ATW = 512
HDIM = 64
BLK = 128
GRID_W = 64
DFF = 2816
NCOL = 5376
EPS = 1e-6
ROPE_THETA = 10000.0
BF16_SUBLANES = 16

C_FB, C_INP, C_QHG, C_FF = 0, 1, 2, 3
C_GATES = 1
C_GHG, C_QRAW = 8, 9
C_KV = 20
C_QKV = 6

ADAM_LR, ADAM_B1, ADAM_B2, ADAM_EPS, ADAM_WD, ADAM_STEP = 0.001, 0.9, 0.999, 1e-08, 0.01, 10

NN = (((1,), (0,)), ((), ()))
NT = (((1,), (1,)), ((), ()))
TN = (((0,), (0,)), ((), ()))


def _dot(a, b, dims=NN, prec=None):
    return lax.dot_general(a, b, dims, precision=prec, preferred_element_type=F32)


def _bdot(a, b, dims=NN):
    return _dot(a.astype(BF16), b.astype(BF16), dims)


def _sig(x):
    return 1.0 / (1.0 + jnp.exp(-x))


class _Carry:
    def __init__(self, ins, outs, aliases, scratch, phases, peers):
        self.ins, self.outs, self.aliases, self.scratch, self.phases = ins, outs, aliases, scratch, phases
        self.peers = peers


BARRIER_IDS = {"sib": 1, "chips": 2, "both": 3}


def _peer_barrier(kind):
    x, y, c = _place()
    peers = []
    if kind in ("sib", "both"):
        peers.append((x, y, 1 - c))
    if kind in ("chips", "both"):
        peers += [(1 - x, y, c), (x, 1 - y, c), (1 - x, 1 - y, c)]
    bar = pltpu.get_barrier_semaphore()
    for peer in peers:
        pl.semaphore_signal(bar, inc=1, device_id=peer, device_id_type=MESH)
    pl.semaphore_wait(bar, len(peers))


def _in_hbm(args):
    return [pltpu.with_memory_space_constraint(a, pltpu.HBM) for a in args]


def _out_hbm(shapes):
    if isinstance(shapes, (list, tuple)):
        return [pltpu.HBM(s.shape, s.dtype) for s in shapes]
    return pltpu.HBM(shapes.shape, shapes.dtype)


def _carry_join(a, b):
    na_in, na_out, na_sc = len(a.ins), len(a.outs), len(a.scratch)
    aliases = dict(a.aliases)
    aliases.update({na_in + i: na_out + o for i, o in b.aliases.items()})

    def phases(ins, outs, sems):
        pa = a.phases(ins[:na_in], outs[:na_out], sems[:na_sc])
        pb = b.phases(ins[na_in:], outs[na_out:], sems[na_sc:])

        def both(fa, fb):
            if fa is None and fb is None:
                return None

            def run():
                for fn in (fa, fb):
                    if fn is not None:
                        fn()
            return run

        return tuple(both(fa, fb) for fa, fb in zip(pa, pb))

    return _Carry(list(a.ins) + list(b.ins), list(a.outs) + list(b.outs), aliases,
                  list(a.scratch) + list(b.scratch), phases, a.peers if a.peers == b.peers else "both")


def _pcall(body, *, name, grid, in_specs, out_specs, out_shape, scratch=(), aliases=None, vmem_mb=48,
           carry=None):
    params = pltpu.CompilerParams(dimension_semantics=("arbitrary",) * len(grid),
                                  vmem_limit_bytes=vmem_mb << 20)
    if carry is None:
        plain = pl.pallas_call(
            body, name=name, grid=grid, in_specs=in_specs, out_specs=out_specs, out_shape=_out_hbm(out_shape),
            scratch_shapes=list(scratch), input_output_aliases=aliases or {}, compiler_params=params)
        return lambda *args: plain(*_in_hbm(args))
    single = not isinstance(out_shape, (list, tuple))
    out_specs_l = [out_specs] if single else list(out_specs)
    out_shape_l = [out_shape] if single else list(out_shape)
    n_in, n_out, n_sc = len(in_specs), len(out_shape_l), len(scratch)
    k_in, k_out = len(carry.ins), len(carry.outs)
    nsteps = int(np.prod(grid))
    assert nsteps >= 3

    def wrapped(*refs):
        ins, cins = refs[:n_in], refs[n_in:n_in + k_in]
        o0 = n_in + k_in
        outs, couts = refs[o0:o0 + n_out], refs[o0 + n_out:o0 + n_out + k_out]
        s0 = o0 + n_out + k_out
        sc, csc = refs[s0:s0 + n_sc], refs[s0 + n_sc:]
        step = pl.program_id(0)
        for ax in range(1, len(grid)):
            step = step * grid[ax] + pl.program_id(ax)
        start, mid, end = carry.phases(cins, couts, csc)

        @pl.when(step == 0)
        def _():
            _peer_barrier(carry.peers)
            start()

        body(*ins, *outs, *sc)
        if mid is not None:
            pl.when(step == nsteps - 2)(mid)
        pl.when(step == nsteps - 1)(end)

    all_aliases = dict(aliases or {})
    all_aliases.update({n_in + i: n_out + o for i, o in carry.aliases.items()})
    call = pl.pallas_call(
        wrapped, name=name, grid=grid, in_specs=list(in_specs) + [ANY] * k_in,
        out_specs=out_specs_l + [ANY] * k_out, out_shape=_out_hbm(out_shape_l + list(carry.outs)),
        scratch_shapes=list(scratch) + list(carry.scratch), input_output_aliases=all_aliases,
        compiler_params=pltpu.CompilerParams(dimension_semantics=("arbitrary",) * len(grid),
                                             vmem_limit_bytes=vmem_mb << 20,
                                             collective_id=BARRIER_IDS[carry.peers]))

    def run(*args):
        res = call(*_in_hbm(args), *_in_hbm(carry.ins))
        core = res[:n_out]
        return (core[0] if single else list(core)), list(res[n_out:])

    return run


def _full(shape):
    nd = len(shape)
    return pl.BlockSpec(shape, lambda *_: (0,) * nd)


ANY = pl.BlockSpec(memory_space=pl.ANY)


NT_IN = NCOL // 256


def _src_block(j):
    return j + jnp.where(j < 4, 2, jnp.where(j < 6, 3, jnp.where(j < 8, -6, jnp.where(
        j < 16, 5, jnp.where(j < 20, -7, -14)))))


def _mm_in(h, wt, tm, carry=None):
    tt = h.shape[0]

    def body(h_ref, w_ref, o_ref):
        o_ref[...] = _bdot(h_ref[...], w_ref[...], NT)

    return _pcall(body, name="mm_in", grid=(tt // tm, NT_IN),
                  in_specs=[pl.BlockSpec((tm, D), lambda i, j: (i, 0)),
                            pl.BlockSpec((256, D), lambda i, j: (_src_block(j), 0))],
                  out_specs=pl.BlockSpec((tm, 256), lambda i, j: (i, j)),
                  out_shape=jax.ShapeDtypeStruct((tt, NCOL), F32), carry=carry)(h, wt)


def _mm_dh(dp, wt, tm, carry=None):
    tt = dp.shape[0]
    per, ng = 3, NT_IN // 3

    def body(d_ref, w0, w1, w2, o_ref, acc):
        kk = pl.program_id(1)

        @pl.when(kk == 0)
        def _():
            acc[...] = jnp.zeros_like(acc)

        acc[...] += (_bdot(d_ref[:, 0:256], w0[...]) + _bdot(d_ref[:, 256:512], w1[...])
                     + _bdot(d_ref[:, 512:768], w2[...]))

        @pl.when(kk == ng - 1)
        def _():
            o_ref[...] = acc[...]

    wspecs = [pl.BlockSpec((256, D), functools.partial(lambda t, i, kk: (_src_block(per * kk + t), 0), t))
              for t in range(per)]
    return _pcall(body, name="mm_dh", grid=(tt // tm, ng),
                  in_specs=[pl.BlockSpec((tm, per * 256), lambda i, kk: (i, kk))] + wspecs,
                  out_specs=pl.BlockSpec((tm, D), lambda i, kk: (i, 0)),
                  out_shape=jax.ShapeDtypeStruct((tt, D), F32), scratch=[pltpu.VMEM((tm, D), F32)],
                  carry=carry)(dp, wt, wt, wt)


def _mm_gin(dp, h, tk):
    tt = dp.shape[0]
    nk = tt // tk

    def body(d_ref, h_ref, o_ref, acc):
        kk = pl.program_id(1)

        @pl.when(kk == 0)
        def _():
            acc[...] = jnp.zeros_like(acc)

        acc[...] += _bdot(d_ref[...], h_ref[...], TN)

        @pl.when(kk == nk - 1)
        def _():
            o_ref[...] = acc[...].astype(BF16)

    return _pcall(body, name="mm_gin", grid=(NT_IN, nk),
                  in_specs=[pl.BlockSpec((tk, 256), lambda j, kk: (kk, j)),
                            pl.BlockSpec((tk, D), lambda j, kk: (kk, 0))],
                  out_specs=pl.BlockSpec((256, D), lambda j, kk: (_src_block(j), 0)),
                  out_shape=jax.ShapeDtypeStruct((NCOL, D), BF16), scratch=[pltpu.VMEM((256, D), F32)])(dp, h)


def _tok_specs():
    assert L == TM
    return [_full((TM, D)), pl.BlockSpec((TM, D), lambda i: (jnp.maximum(i - 1, 0), 0))]


def _mod1(ctx, x, nw, ss):
    rows = L + x.shape[0]

    def body(c_ref, x_ref, nw_ref, ss_ref, h_ref):
        t = jnp.where(pl.program_id(0) == 0, c_ref[...], x_ref[...])
        r = lax.rsqrt(jnp.mean(t * t, axis=-1, keepdims=True) + EPS)
        s = ss_ref[0]
        h_ref[...] = ((t * r * nw_ref[...]) * (1.0 + s[1:2]) + s[0:1]).astype(BF16)

    return _pcall(body, name="mod1", grid=(rows // TM,),
                  in_specs=_tok_specs() + [_full((1, D)),
                                           pl.BlockSpec((1, 2, D), lambda i: (jnp.minimum(i, 1), 0, 0))],
                  out_specs=pl.BlockSpec((TM, D), lambda i: (i, 0)),
                  out_shape=jax.ShapeDtypeStruct((rows, D), BF16))(ctx, x, nw, ss)


def _norm_bwd_rows(x, dh, nw, scale):
    r = lax.rsqrt(jnp.mean(x * x, axis=-1, keepdims=True) + EPS)
    xh = x * r
    dxh = dh * ((1.0 + scale) * nw)
    dx = r * (dxh - xh * jnp.mean(dxh * xh, axis=-1, keepdims=True))
    return dx, xh


def _out_proj_mod2(mixed, w_o, x, g1, nw2, ss2):
    s_len = x.shape[0]
    tm = 512

    def body(m_ref, w_ref, x_ref, g_ref, nw_ref, ss_ref, ao_ref, x1_ref, h_ref):
        ao = _bdot(m_ref[...], w_ref[...])
        ao_ref[...] = ao.astype(BF16)
        x1 = x_ref[...] + g_ref[...] * ao
        x1_ref[...] = x1
        r = lax.rsqrt(jnp.mean(x1 * x1, axis=-1, keepdims=True) + EPS)
        s = ss_ref[0]
        h_ref[...] = ((x1 * r * nw_ref[...]) * (1.0 + s[1:2]) + s[0:1]).astype(BF16)

    row = pl.BlockSpec((tm, D), lambda i: (i, 0))
    f = jax.ShapeDtypeStruct((s_len, D), F32)
    return _pcall(body, name="out_proj_mod2", grid=(s_len // tm,),
                  in_specs=[row, _full((D, D)), row, _full((1, D)), _full((1, D)), _full((1, 2, D))],
                  out_specs=[row, row, row],
                  out_shape=[jax.ShapeDtypeStruct((s_len, D), BF16), f,
                             jax.ShapeDtypeStruct((s_len, D), BF16)])(mixed, w_o, x, g1, nw2, ss2)


TS = 1024


def _acc_call(body, *, name, grid, in_specs, out_specs, out_shape, acc_shapes, args, carry=None):
    return _pcall(body, name=name, grid=grid, in_specs=in_specs, out_specs=out_specs, out_shape=out_shape,
                  scratch=[pltpu.VMEM(s, F32) for s in acc_shapes], carry=carry)(*args)


def _ffn_up(h2, g4, u4, carry=None):
    s_len = h2.shape[0]
    ns = g4.shape[1]

    def body(h_ref, g_ref, u_ref, a_ref, b_ref, z_ref):
        h = h_ref[...]
        a = _bdot(h, g_ref[0], NT)
        b = _bdot(h, u_ref[0], NT)
        a_ref[0] = a.astype(BF16)
        b_ref[0] = b.astype(BF16)
        z_ref[0] = (a * _sig(a) * b).astype(BF16)

    w = pl.BlockSpec((1, ns, D), lambda i, j: (j, 0, 0))
    o = pl.BlockSpec((1, TS, ns), lambda i, j: (j, i, 0))
    f = jax.ShapeDtypeStruct((4, s_len, ns), BF16)
    return _pcall(body, name="ffn_up", grid=(s_len // TS, 4),
                  in_specs=[pl.BlockSpec((TS, D), lambda i, j: (i, 0)), w, w], out_specs=[o, o, o],
                  out_shape=[f, f, jax.ShapeDtypeStruct((4, s_len, ns), BF16)], carry=carry)(h2, g4, u4)


def _ffn_down_loss(z4, dn4, x1, g2, tgt):
    _, s_len, ns = z4.shape

    def body(z_ref, w_ref, x1_ref, g_ref, t_ref, sq_ref, dx2_ref, dyb_ref, dg_ref, acc):
        i, j = pl.program_id(0), pl.program_id(1)

        @pl.when((i == 0) & (j == 0))
        def _():
            sq_ref[...] = jnp.zeros_like(sq_ref)
            dg_ref[...] = jnp.zeros_like(dg_ref)

        @pl.when(j == 0)
        def _():
            acc[...] = jnp.zeros_like(acc)

        acc[...] += _bdot(z_ref[0], w_ref[0])

        @pl.when(j == 3)
        def _():
            y_ = acc[...]
            g = g_ref[...]
            e = x1_ref[...] + g * y_ - t_ref[...]
            sq_ref[...] += jnp.sum(e * e, axis=0, keepdims=True)
            dx2 = e * (1.0 / D)
            dx2_ref[...] = dx2
            dyb_ref[...] = (g * dx2).astype(BF16)
            dg_ref[...] += jnp.sum(dx2 * y_, axis=0, keepdims=True)

    row = pl.BlockSpec((TS, D), lambda i, j: (i, 0))
    vec = _full((1, D))
    return _acc_call(body, name="ffn_down_loss", grid=(s_len // TS, 4),
                     in_specs=[pl.BlockSpec((1, TS, ns), lambda i, j: (j, i, 0)),
                               pl.BlockSpec((1, ns, D), lambda i, j: (j, 0, 0)), row, vec, row],
                     out_specs=[vec, row, row, vec],
                     out_shape=[jax.ShapeDtypeStruct((1, D), F32), jax.ShapeDtypeStruct((s_len, D), F32),
                                jax.ShapeDtypeStruct((s_len, D), BF16), jax.ShapeDtypeStruct((1, D), F32)],
                     acc_shapes=[(TS, D)], args=(z4, dn4, x1, g2, tgt))


def _ffn_dz(dyb, dn4, a4, b4):
    _, s_len, ns = a4.shape

    def body(dy_ref, w_ref, a_ref, b_ref, da_ref, db_ref):
        dz = _bdot(dy_ref[...], w_ref[0], NT)
        a = a_ref[0].astype(F32)
        s = _sig(a)
        da_ref[0] = (dz * b_ref[0].astype(F32) * (s * (1.0 + a * (1.0 - s)))).astype(BF16)
        db_ref[0] = (dz * (a * s)).astype(BF16)

    t = pl.BlockSpec((1, TS, ns), lambda i, j: (j, i, 0))
    o = jax.ShapeDtypeStruct((4, s_len, ns), BF16)
    return _pcall(body, name="ffn_dz", grid=(s_len // TS, 4),
                  in_specs=[pl.BlockSpec((TS, D), lambda i, j: (i, 0)),
                            pl.BlockSpec((1, ns, D), lambda i, j: (j, 0, 0)), t, t],
                  out_specs=[t, t], out_shape=[o, o])(dyb, dn4, a4, b4)


def _ffn_gdn(z4, dyb):
    _, s_len, ns = z4.shape
    tk = min(s_len, 2 * TS)
    nk = s_len // tk

    def body(z_ref, dy_ref, o_ref, acc):
        t = pl.program_id(1)

        @pl.when(t == 0)
        def _():
            acc[...] = jnp.zeros_like(acc)

        acc[...] += _bdot(z_ref[0], dy_ref[...], TN)

        @pl.when(t == nk - 1)
        def _():
            o_ref[0] = acc[...].astype(o_ref.dtype)

    return _acc_call(body, name="ffn_gdn", grid=(4, nk),
                     in_specs=[pl.BlockSpec((1, tk, ns), lambda j, t: (j, t, 0)),
                               pl.BlockSpec((tk, D), lambda j, t: (t, 0))],
                     out_specs=pl.BlockSpec((1, ns, D), lambda j, t: (j, 0, 0)),
                     out_shape=jax.ShapeDtypeStruct((4, ns, D), BF16), acc_shapes=[(ns, D)], args=(z4, dyb))


def _ffn_dh2(da4, db4, g4, u4, carry=None):
    _, s_len, ns = da4.shape

    def body(da_ref, db_ref, g_ref, u_ref, o_ref, acc):
        j = pl.program_id(1)

        @pl.when(j == 0)
        def _():
            acc[...] = jnp.zeros_like(acc)

        acc[...] += _bdot(da_ref[0], g_ref[0]) + _bdot(db_ref[0], u_ref[0])

        @pl.when(j == 3)
        def _():
            o_ref[...] = acc[...]

    t = pl.BlockSpec((1, TS, ns), lambda i, j: (j, i, 0))
    w = pl.BlockSpec((1, ns, D), lambda i, j: (j, 0, 0))
    return _acc_call(body, name="ffn_dh2", grid=(s_len // TS, 4), in_specs=[t, t, w, w],
                     out_specs=pl.BlockSpec((TS, D), lambda i, j: (i, 0)),
                     out_shape=jax.ShapeDtypeStruct((s_len, D), F32), acc_shapes=[(TS, D)],
                     args=(da4, db4, g4, u4), carry=carry)


def _ffn_ggu(h2, da4, db4, carry=None):
    _, s_len, ns = da4.shape
    nk = s_len // TS

    def body(h_ref, da_ref, db_ref, gg_ref, gu_ref, acc_g, acc_u):
        t = pl.program_id(1)

        @pl.when(t == 0)
        def _():
            acc_g[...] = jnp.zeros_like(acc_g)
            acc_u[...] = jnp.zeros_like(acc_u)

        h = h_ref[...]
        acc_g[...] += _bdot(da_ref[0], h, TN)
        acc_u[...] += _bdot(db_ref[0], h, TN)

        @pl.when(t == nk - 1)
        def _():
            gg_ref[0] = acc_g[...].astype(BF16)
            gu_ref[0] = acc_u[...].astype(BF16)

    d = pl.BlockSpec((1, TS, ns), lambda j, t: (j, t, 0))
    o = pl.BlockSpec((1, ns, D), lambda j, t: (j, 0, 0))
    f = jax.ShapeDtypeStruct((4, ns, D), BF16)
    return _acc_call(body, name="ffn_ggu", grid=(4, nk),
                     in_specs=[pl.BlockSpec((TS, D), lambda j, t: (t, 0)), d, d], out_specs=[o, o],
                     out_shape=[f, f], acc_shapes=[(ns, D), (ns, D)], args=(h2, da4, db4), carry=carry)


def _mod2_bwd(x1, dh2, dx2, ao, nw2, ss2, g1):
    s_len = x1.shape[0]

    def body(x1_ref, dh_ref, dx2_ref, ao_ref, nw_ref, ss_ref, g_ref,
             dx1_ref, da_ref, dss_ref, dnw_ref, dg_ref):
        i = pl.program_id(0)

        @pl.when(i == 0)
        def _():
            dss_ref[...] = jnp.zeros_like(dss_ref)
            dnw_ref[...] = jnp.zeros_like(dnw_ref)
            dg_ref[...] = jnp.zeros_like(dg_ref)

        dh = dh_ref[...]
        nw = nw_ref[...]
        scale = ss_ref[0][1:2]
        dxn, xh = _norm_bwd_rows(x1_ref[...], dh, nw, scale)
        dx1 = dx2_ref[...] + dxn
        dx1_ref[...] = dx1
        da_ref[...] = (g_ref[...] * dx1).astype(BF16)
        dg_ref[...] += jnp.sum(dx1 * ao_ref[...].astype(F32), axis=0, keepdims=True)
        dsh = jnp.sum(dh, axis=0, keepdims=True)
        dsc = jnp.sum(dh * xh * nw, axis=0, keepdims=True)
        dss_ref[...] += jnp.concatenate([dsh, dsc], axis=0)
        dnw_ref[...] += jnp.sum(dh * xh * (1.0 + scale), axis=0, keepdims=True)

    row = pl.BlockSpec((TM, D), lambda i: (i, 0))
    vec = _full((1, D))
    return _pcall(body, name="mod2_bwd", grid=(s_len // TM,),
                  in_specs=[row, row, row, row, vec, _full((1, 2, D)), vec],
                  out_specs=[row, row, _full((2, D)), vec, vec],
                  out_shape=[jax.ShapeDtypeStruct((s_len, D), F32), jax.ShapeDtypeStruct((s_len, D), BF16),
                             jax.ShapeDtypeStruct((2, D), F32), jax.ShapeDtypeStruct((1, D), F32),
                             jax.ShapeDtypeStruct((1, D), F32)])(x1, dh2, dx2, ao, nw2, ss2, g1)


def _mod1_bwd(ctx, x, dh, dx1, nw1, ss1):
    s_len = dx1.shape[0]
    tt = L + s_len

    def body(c_ref, x_ref, dh_ref, dx1_ref, nw_ref, ss_ref, dx_ref, dss_ref, dnw_ref):
        i = pl.program_id(0)
        tok = jnp.where(i == 0, c_ref[...], x_ref[...])

        @pl.when(i == 0)
        def _():
            dnw_ref[...] = jnp.zeros_like(dnw_ref)

        @pl.when(i <= 1)
        def _():
            dss_ref[...] = jnp.zeros_like(dss_ref)

        dh_ = dh_ref[...]
        nw = nw_ref[...]
        scale = ss_ref[0][1:2]
        dxn, xh = _norm_bwd_rows(tok, dh_, nw, scale)

        @pl.when(i >= 1)
        def _():
            dx_ref[...] = dx1_ref[...] + dxn

        dsh = jnp.sum(dh_, axis=0, keepdims=True)
        dsc = jnp.sum(dh_ * xh * nw, axis=0, keepdims=True)
        dss_ref[...] += jnp.concatenate([dsh, dsc], axis=0)[None]
        dnw_ref[...] += jnp.sum(dh_ * xh * (1.0 + scale), axis=0, keepdims=True)

    row = pl.BlockSpec((TM, D), lambda i: (i, 0))
    lat = pl.BlockSpec((TM, D), lambda i: (jnp.maximum(i - 1, 0), 0))
    sel = pl.BlockSpec((1, 2, D), lambda i: (jnp.minimum(i, 1), 0, 0))
    return _pcall(body, name="mod1_bwd", grid=(tt // TM,),
                  in_specs=_tok_specs() + [row, lat, _full((1, D)), sel],
                  out_specs=[lat, sel, _full((1, D))],
                  out_shape=[jax.ShapeDtypeStruct((s_len, D), F32), jax.ShapeDtypeStruct((2, 2, D), F32),
                             jax.ShapeDtypeStruct((1, D), F32)])(ctx, x, dh, dx1, nw1, ss1)


def _rows(c):
    return slice(c * CH, (c + 1) * CH)


def _chunk_masks(rev, transpose=False):
    r = lax.broadcasted_iota(jnp.int32, (TM, TM), 0)
    c = lax.broadcasted_iota(jnp.int32, (TM, TM), 1)
    same = (r // CH) == (c // CH)
    before = (c >= r) if (rev != transpose) else (c <= r)
    return same & before, same


def _chunk_scan(x, rev, transpose=False):
    r = lax.broadcasted_iota(jnp.int32, (CH, CH), 0)
    c = lax.broadcasted_iota(jnp.int32, (CH, CH), 1)
    tri = ((c >= r) if (rev != transpose) else (c <= r)).astype(F32)
    return jnp.concatenate([_dot(tri, x[_rows(ch)], prec=HI) for ch in range(x.shape[0] // CH)], axis=0)


def _chunk_total(x):
    return jnp.concatenate([jnp.broadcast_to(jnp.sum(x[_rows(ch)], axis=0, keepdims=True), (CH, x.shape[1]))
                            for ch in range(x.shape[0] // CH)], axis=0)


def _hgrn_gate(fl, qraw, lg):
    lb = 1.0 / (1.0 + jnp.exp(lg[1:2] - lg[0:1]))
    sg = _sig(fl)
    f = lb + (1.0 - lb) * sg
    q = qraw * _sig(qraw) * (HGD ** -0.5)
    return lb, sg, f, q


def _hgrn_fwd(p, lg, *, rev, carry=None, readout=None):
    tt = p.shape[0]
    nt = tt // TM
    ncht = TM // CH
    d = 1 if rev else 0

    def tile_of(s):
        return jnp.where(s == 0, 0, nt - s) if rev else s

    def body(*refs):
        if readout is None:
            f_ref, inp_ref, q_ref, lg_ref, o_ref, st_ref, state = refs
        else:
            f_ref, inp_ref, q_ref, lg_ref, oo_ref, g_ref, hw_ref, o_ref, st_ref, y_ref, state = refs
        s = pl.program_id(0)

        @pl.when(s == 0)
        def _():
            state[...] = jnp.zeros_like(state)

        _, _, f, q = _hgrn_gate(f_ref[...], q_ref[...], lg_ref[0])
        lf = jnp.log(f)
        causal, _ = _chunk_masks(rev)
        cum = _chunk_scan(lf, rev)
        tot = _chunk_total(lf)
        qd = (q * jnp.exp(cum)).astype(BF16)
        kd = ((1.0 - f) * jnp.exp(-cum)).astype(BF16)
        ke = ((1.0 - f) * jnp.exp(tot - cum)).astype(BF16)
        et = jnp.exp(tot)
        v = inp_ref[...].astype(BF16)
        order = range(ncht - 1, -1, -1) if rev else range(ncht)
        outs = []
        for h in range(4):
            sl = slice(h * HGD, (h + 1) * HGD)
            qd_, kd_, ke_, v_ = qd[:, sl], kd[:, sl], ke[:, sl], v[:, sl]
            pm = jnp.where(causal, _dot(qd_, kd_, NT), 0.0).astype(BF16)
            o_h = _dot(pm, v_)
            upd = [_dot(v_[_rows(c)], ke_[_rows(c)], TN) for c in range(ncht)]
            st = state[h]
            for c in order:
                st_ref[c, h] = st
                st = st * et[c * CH:c * CH + 1, sl] + upd[c]
            state[h] = st
            inter = [_dot(qd_[_rows(c)], st_ref[c, h].astype(BF16), NT) for c in range(ncht)]
            outs.append(o_h + jnp.concatenate(inter, axis=0))
        o_tile = jnp.concatenate(outs, axis=1)
        o_ref[...] = o_tile
        if readout is not None:
            @pl.when(tile_of(s) >= 1)
            def _():
                g = g_ref[...]
                y_ref[...] = (_head_rms(oo_ref[...] + o_tile, None, 4) * hw_ref[...] * (g * _sig(g))).astype(BF16)

    def col(cb):
        return pl.BlockSpec((TM, HGW), lambda s: (tile_of(s), cb))

    in_specs = [col(C_FB if rev else C_FF), col(C_INP), col(C_QHG), pl.BlockSpec((1, 2, HGW), lambda s: (d, 0, 0))]
    out_specs = [col(0), pl.BlockSpec((ncht, 4, HGD, HGD), lambda s: (tile_of(s), 0, 0, 0))]
    out_shape = [jax.ShapeDtypeStruct((tt, HGW), F32), jax.ShapeDtypeStruct((nt * ncht, 4, HGD, HGD), F32)]
    args = [p, p, p, lg]
    if readout is not None:
        in_specs += [col(0), col(C_GHG), _full((1, HGW))]
        args += [readout[0], p, readout[1]]
        assert rev
        out_specs.append(pl.BlockSpec((TM, HGW), lambda s: (jnp.where(s == 0, nt - 2, tile_of(s) - 1), 0)))
        out_shape.append(jax.ShapeDtypeStruct((tt - L, HGW), BF16))
    return _pcall(body, name="hgrn_fwd_rev" if rev else "hgrn_fwd", grid=(nt,), in_specs=in_specs,
                  out_specs=out_specs, out_shape=out_shape, scratch=[pltpu.VMEM((4, HGD, HGD), F32)],
                  carry=carry)(*args)


def _hgrn_bwd(p, lg, do, st, dp, prev, *, rev, carry=None):
    tt = p.shape[0]
    nt = tt // TM
    ncht = TM // CH
    d = 1 if rev else 0
    second = prev is not None

    def tile_of(s):
        return jnp.where(s == nt - 1, 0, s + 1) if rev else nt - 1 - s

    def body(*refs):
        if second:
            (f_ref, inp_ref, q_ref, lg_ref, do_ref, st_ref, dvp_ref, dqp_ref, _dp_in,
             dp_ref, dlg_ref, dstate) = refs
        else:
            (f_ref, inp_ref, q_ref, lg_ref, do_ref, st_ref, _dp_in,
             dp_ref, dv_ref, dq_ref, dlg_ref, dstate) = refs
        s = pl.program_id(0)
        tile = tile_of(s)

        @pl.when(s == 0)
        def _():
            dstate[...] = jnp.zeros_like(dstate)
            dlg_ref[...] = jnp.zeros_like(dlg_ref)

        qraw = q_ref[...]
        lb, sg, f, q = _hgrn_gate(f_ref[...], qraw, lg_ref[0])
        lf = jnp.log(f)
        causal, _ = _chunk_masks(rev)
        causal_t, _ = _chunk_masks(rev, transpose=True)
        cum = _chunk_scan(lf, rev)
        tot = _chunk_total(lf)
        ea, eb, ee, et = jnp.exp(cum), jnp.exp(-cum), jnp.exp(tot - cum), jnp.exp(tot)
        qdf, kdf, kef = q * ea, (1.0 - f) * eb, (1.0 - f) * ee
        qd, kd, ke = qdf.astype(BF16), kdf.astype(BF16), kef.astype(BF16)
        v = inp_ref[...].astype(BF16)
        dob = jnp.where(tile == 0, 0.0, do_ref[...]).astype(BF16)
        order = range(ncht) if rev else range(ncht - 1, -1, -1)
        dq_l, dk_l, dv_l, dcum_l, dtot_l = [], [], [], [], []
        for h in range(4):
            sl = slice(h * HGD, (h + 1) * HGD)
            qd_, kd_, ke_, v_, do_ = qd[:, sl], kd[:, sl], ke[:, sl], v[:, sl], dob[:, sl]
            pmt = jnp.where(causal_t, _dot(kd_, qd_, NT), 0.0).astype(BF16)
            dpm = jnp.where(causal, _dot(do_, v_, NT), 0.0).astype(BF16)
            dpmt = jnp.where(causal_t, _dot(v_, do_, NT), 0.0).astype(BF16)
            dv = _dot(pmt, do_)
            dqd = _dot(dpm, kd_)
            dkd = _dot(dpmt, qd_)
            upd = [_dot(do_[_rows(c)], qd_[_rows(c)], TN) for c in range(ncht)]
            ds = dstate[h]
            ds1 = [None] * ncht
            for c in order:
                ds1[c] = ds
                ds = ds * et[c * CH:c * CH + 1, sl] + upd[c]
            dstate[h] = ds
            dke_c, dv_c, dqd_c, dtot_c = [], [], [], []
            for c in range(ncht):
                st0 = st_ref[c, h]
                dsb = ds1[c].astype(BF16)
                dke_ = _dot(v_[_rows(c)], dsb)
                dke_c.append(dke_)
                dv_c.append(_dot(ke_[_rows(c)], dsb, NT))
                dqd_c.append(_dot(do_[_rows(c)], st0.astype(BF16)))
                dt = (jnp.sum(ds1[c] * st0, axis=0, keepdims=True) * et[c * CH:c * CH + 1, sl]
                      + jnp.sum(dke_ * kef[_rows(c), sl], axis=0, keepdims=True))
                dtot_c.append(jnp.broadcast_to(dt, (CH, HGD)))
            dke = jnp.concatenate(dke_c, axis=0)
            dqd = dqd + jnp.concatenate(dqd_c, axis=0)
            dv_l.append(dv + jnp.concatenate(dv_c, axis=0))
            dtot_l.append(jnp.concatenate(dtot_c, axis=0))
            dq_l.append(dqd * ea[:, sl])
            dk_l.append(dkd * eb[:, sl] + dke * ee[:, sl])
            dcum_l.append(dqd * qdf[:, sl] - dkd * kdf[:, sl] - dke * kef[:, sl])
        dcum = jnp.concatenate(dcum_l, axis=1)
        dlf = _chunk_scan(dcum, rev, transpose=True) + jnp.concatenate(dtot_l, axis=1)
        dq_t = jnp.concatenate(dq_l, axis=1)
        dv_t = jnp.concatenate(dv_l, axis=1)

        df = dlf / f - jnp.concatenate(dk_l, axis=1)
        dfl = df * (1.0 - lb) * sg * (1.0 - sg)
        dlb = jnp.sum(df * (1.0 - sg), axis=0, keepdims=True)
        dl0 = dlb * lb * (1.0 - lb)
        dlg_ref[...] += jnp.concatenate([dl0, -dl0], axis=0)[None]
        if second:
            sq = _sig(qraw)
            dqr = (dqp_ref[...] + dq_t) * (HGD ** -0.5) * (sq * (1.0 + qraw * (1.0 - sq)))
            dp_ref[...] = jnp.concatenate([dfl, dvp_ref[...] + dv_t, dqr], axis=1).astype(BF16)
        else:
            dp_ref[...] = dfl.astype(BF16)
            dv_ref[...] = dv_t
            dq_ref[...] = dq_t

    def col(cb):
        return pl.BlockSpec((TM, HGW), lambda s: (tile_of(s), cb))

    tok = pl.BlockSpec((TM, HGW), lambda s: (tile_of(s), 0))
    in_specs = [col(C_FB if rev else C_FF), col(C_INP), col(C_QHG),
                pl.BlockSpec((1, 2, HGW), lambda s: (d, 0, 0)),
                pl.BlockSpec((TM, HGW), lambda s: (jnp.maximum(tile_of(s) - 1, 0), 0)),
                pl.BlockSpec((ncht, 4, HGD, HGD), lambda s: (tile_of(s), 0, 0, 0))]
    args = [p, p, p, lg, do, st]
    dlg_spec = _full((1, 2, HGW))
    dlg_shape = jax.ShapeDtypeStruct((1, 2, HGW), F32)
    if second:
        in_specs += [tok, tok]
        args += [prev[0], prev[1]]
        out_specs = [pl.BlockSpec((TM, 3 * HGW), lambda s: (tile_of(s), 0)), dlg_spec]
        out_shape = [jax.ShapeDtypeStruct(dp.shape, BF16), dlg_shape]
    else:
        out_specs = [pl.BlockSpec((TM, HGW), lambda s: (tile_of(s), C_FB if rev else C_FF)), tok, tok, dlg_spec]
        out_shape = [jax.ShapeDtypeStruct(dp.shape, BF16), jax.ShapeDtypeStruct((tt, HGW), F32),
                     jax.ShapeDtypeStruct((tt, HGW), F32), dlg_shape]
    in_specs.append(ANY)
    args.append(dp)
    return _pcall(body, name="hgrn_bwd_rev" if rev else "hgrn_bwd", grid=(nt,),
                  in_specs=in_specs, out_specs=out_specs, out_shape=out_shape,
                  scratch=[pltpu.VMEM((4, HGD, HGD), F32)],
                  aliases={len(args) - 1: 0}, carry=carry)(*args)


def _head_rms(o, w, nheads):
    outs = []
    for h in range(nheads):
        oh = o[:, h * HGD:(h + 1) * HGD]
        outs.append(oh * lax.rsqrt(jnp.mean(oh * oh, axis=-1, keepdims=True) + EPS))
    return jnp.concatenate(outs, axis=1)


def _readout_bwd(o0, o1, p, hw4, dy, dp, carry=None):
    tt = o0.shape[0]
    s_len = tt - L

    def body(o0_ref, o1_ref, g_ref, w_ref, dy_ref, _dp_in, dp_ref, do_ref, dw_ref):
        i = pl.program_id(0)

        @pl.when(i == 0)
        def _():
            dw_ref[...] = jnp.zeros_like(dw_ref)
            dp_ref[...] = jnp.zeros_like(dp_ref)

        @pl.when(i >= 1)
        def _():
            o = o0_ref[...] + o1_ref[...]
            g = g_ref[...]
            w = w_ref[...]
            sg = _sig(g)
            dy_ = dy_ref[...]
            dsw = dy_ * (g * sg)
            outs, xhs = [], []
            for h in range(4):
                sl = slice(h * HGD, (h + 1) * HGD)
                oh = o[:, sl]
                r = lax.rsqrt(jnp.mean(oh * oh, axis=-1, keepdims=True) + EPS)
                xh = oh * r
                dxh = dsw[:, sl] * w[:, sl]
                outs.append(r * (dxh - xh * jnp.mean(dxh * xh, axis=-1, keepdims=True)))
                xhs.append(xh)
            xh = jnp.concatenate(xhs, axis=1)
            do_ref[...] = jnp.concatenate(outs, axis=1)
            dp_ref[...] = (dy_ * xh * w * (sg * (1.0 + g * (1.0 - sg)))).astype(BF16)
            dw_ref[...] += jnp.sum(dsw * xh, axis=0, keepdims=True)

    tok = pl.BlockSpec((TM, HGW), lambda i: (i, 0))
    lat = pl.BlockSpec((TM, HGW), lambda i: (jnp.maximum(i - 1, 0), 0))
    return _pcall(body, name="readout_bwd", grid=(tt // TM,),
                  in_specs=[tok, tok, pl.BlockSpec((TM, HGW), lambda i: (i, C_GHG)), _full((1, HGW)), lat, ANY],
                  out_specs=[pl.BlockSpec((TM, HGW), lambda i: (i, C_GHG)), lat, _full((1, HGW))],
                  out_shape=[jax.ShapeDtypeStruct(dp.shape, BF16), jax.ShapeDtypeStruct((s_len, HGW), F32),
                             jax.ShapeDtypeStruct((1, HGW), F32)],
                  aliases={5: 0}, carry=carry)(o0, o1, p, hw4, dy, dp)


def _rope_tables(s_len):
    t = np.arange(s_len)
    inv = ROPE_THETA ** (-np.arange(0, 32, 2, dtype=np.float64) / 32)
    def half(pos):
        ang = pos[:, None].astype(np.float64) * inv[None, :]
        return (np.concatenate([np.cos(ang), np.cos(ang)], 1), np.concatenate([-np.sin(ang), np.sin(ang)], 1))
    cr, sr = half(t // GRID_W)
    cc, sc = half(t % GRID_W)
    cos = np.concatenate([cr, cc, cr, cc], 1)
    sin = np.concatenate([sr, sc, sr, sc], 1)
    cos = np.concatenate([np.ones((L, 128)), cos], 0)
    sin = np.concatenate([np.zeros((L, 128)), sin], 0)
    return jnp.asarray(cos, F32), jnp.asarray(sin, F32)


def _blockdiag(n, w):
    i = np.arange(n)
    return jnp.asarray((i[:, None] // w == i[None, :] // w) / float(w), F32)


def _dup_matrix():
    m = np.zeros((128, 512), np.float32)
    for g in range(2):
        for j in range(4):
            for dd in range(HDIM):
                m[64 * g + dd, 256 * g + 64 * j + dd] = 1.0
    return m


def _head_mean(x, blockdiag):
    return _dot(x, blockdiag, prec=lax.Precision.HIGH)


def _rot(x):
    n = x.shape[1]
    lane = lax.broadcasted_iota(jnp.int32, x.shape, 1)
    return jnp.where((lane % 32) < 16, pltpu.roll(x, n - 16, 1), pltpu.roll(x, 16, 1))


def _qk_prep(p, cos, sin, qnw8, knw2, bd512, bd128, dup):
    tt = p.shape[0]

    def body(q_ref, kv_ref, cos_ref, sin_ref, qw_ref, kw_ref, b5_ref, b1_ref, dup_ref,
             qr_ref, k4_ref, v4_ref):
        cos_, sin_ = cos_ref[...], sin_ref[...]
        q = q_ref[...]
        qn = q * lax.rsqrt(_head_mean(q * q, b5_ref[...]) + EPS) * qw_ref[...]
        cos4 = jnp.concatenate([cos_] * 4, axis=1)
        sin4 = jnp.concatenate([sin_] * 4, axis=1)
        qr_ref[...] = ((qn * cos4 + _rot(qn) * sin4) * (HDIM ** -0.5)).astype(BF16)
        kv = kv_ref[...]
        k, v = kv[:, :128], kv[:, 128:]
        kn = k * lax.rsqrt(_head_mean(k * k, b1_ref[...]) + EPS) * kw_ref[...]
        kr = kn * cos_ + _rot(kn) * sin_
        k4_ref[...] = _bdot(kr, dup_ref[...]).astype(BF16)
        v4_ref[...] = _bdot(v, dup_ref[...]).astype(BF16)

    row = lambda w, cb: pl.BlockSpec((TM, w), lambda i: (i, cb))
    out = jax.ShapeDtypeStruct((tt, ATW), BF16)
    return _pcall(body, name="qk_prep", grid=(tt // TM,),
                  in_specs=[row(ATW, C_QRAW), row(256, C_KV), row(128, 0), row(128, 0),
                            _full((1, ATW)), _full((1, 128)), _full((ATW, ATW)), _full((128, 128)),
                            _full((128, ATW))],
                  out_specs=[row(ATW, 0)] * 3, out_shape=[out] * 3)(
                      p, p, cos, sin, qnw8, knw2, bd512, bd128, dup)


def _attn_masks(i, nb):
    r = lax.broadcasted_iota(jnp.int32, (4 * BLK, 3 * BLK + L), 0) % BLK
    c = lax.broadcasted_iota(jnp.int32, (4 * BLK, 3 * BLK + L), 1)
    kpos = (i - 1) * BLK + c
    loc = (jnp.abs(c - BLK - r) <= BLK) & (kpos >= 0) & (kpos < nb * BLK)
    return loc | (c >= 3 * BLK)


def _stack_mask():
    r = lax.broadcasted_iota(jnp.int32, (4 * BLK, 256), 0)
    lane = lax.broadcasted_iota(jnp.int32, (4 * BLK, 256), 1)
    return (r // BLK) == (lane // HDIM)


def _stack_heads(xg, fill=0.0):
    x4 = jnp.concatenate([xg] * 4, axis=0)
    return jnp.where(_stack_mask(), x4, jnp.full_like(x4, fill))


def _unstack_heads(x4):
    out = jnp.where(_lane_mask(0), x4[0:BLK], 0.0)
    for j in range(1, 4):
        out = out + jnp.where(_lane_mask(j), x4[j * BLK:(j + 1) * BLK], 0.0)
    return out


def _per_head_rows(vals):
    return jnp.concatenate([jnp.broadcast_to(v, (BLK, 1)) for v in vals], axis=0)


def _lane_mask(j):
    lane = lax.broadcasted_iota(jnp.int32, (1, 256), 1)
    return (lane // HDIM) == j


def _attn_specs(nb):
    blk = lambda off: pl.BlockSpec((BLK, ATW), lambda i: (jnp.clip(i + off, 0, nb - 1) + 2, 0))
    ctx = pl.BlockSpec((L, ATW), lambda i: (0, 0))
    return blk, ctx


def _attn_fwd(qr, k4, v4, sinks, carry=None):
    tt = qr.shape[0]
    s_len = tt - L
    nb = s_len // BLK

    def body(sk_ref, q_ref, kp, ko, kn, kc, vp, vo, vn, vc, y_ref, lse_ref):
        i = pl.program_id(0)
        valid = _attn_masks(i, nb)
        q = q_ref[...]
        ys, lses = [], []
        for g in range(2):
            gs = slice(256 * g, 256 * g + 256)
            kcat = jnp.concatenate([kp[:, gs], ko[:, gs], kn[:, gs], kc[:, gs]], axis=0)
            vcat = jnp.concatenate([vp[:, gs], vo[:, gs], vn[:, gs], vc[:, gs]], axis=0)
            sink4 = _per_head_rows([sk_ref[4 * g + j] for j in range(4)])
            q4 = _stack_heads(q[:, gs])
            o_parts, l_parts = [], []
            for hp in range(2):
                rows = slice(2 * BLK * hp, 2 * BLK * (hp + 1))
                sink = sink4[rows]
                s = jnp.where(valid[rows], _dot(q4[rows], kcat, NT), -1e30)
                m = jnp.maximum(jnp.max(s, axis=-1, keepdims=True), sink)
                e = jnp.exp(s - m)
                den = jnp.sum(e, axis=-1, keepdims=True) + jnp.exp(sink - m)
                o_parts.append(_bdot(e * (1.0 / den), vcat))
                l_parts.append(jnp.broadcast_to(m + jnp.log(den), (2 * BLK, 256)))
            ys.append(_unstack_heads(jnp.concatenate(o_parts, axis=0)))
            lses.append(_unstack_heads(jnp.concatenate(l_parts, axis=0)))
        y_ref[...] = jnp.concatenate(ys, axis=1).astype(BF16)
        lse_ref[...] = jnp.concatenate(lses, axis=1)

    blk, ctx = _attn_specs(nb)
    out = pl.BlockSpec((BLK, ATW), lambda i: (i, 0))
    return _pcall(body, name="attn_fwd", grid=(nb,),
                  in_specs=[pl.BlockSpec(memory_space=pltpu.SMEM), blk(0),
                            blk(-1), blk(0), blk(1), ctx, blk(-1), blk(0), blk(1), ctx],
                  out_specs=[out, out],
                  out_shape=[jax.ShapeDtypeStruct((s_len, ATW), BF16),
                             jax.ShapeDtypeStruct((s_len, ATW), F32)], carry=carry)(
                      sinks, qr, k4, k4, k4, k4, v4, v4, v4, v4)


def _attn_bwd(qr, k4, v4, sinks, y, lse, dy, carry=None):
    tt = qr.shape[0]
    s_len = tt - L
    nb = s_len // BLK

    def body(sk_ref, q_ref, kp, ko, kn, kc, vp, vo, vn, vc, y_ref, lse_ref, dy_ref,
             dq_ref, dkw_ref, dvw_ref, dkc_ref, dvc_ref, dsk_ref):
        i = pl.program_id(0)

        @pl.when(i == 0)
        def _():
            dkc_ref[...] = jnp.zeros_like(dkc_ref)
            dvc_ref[...] = jnp.zeros_like(dvc_ref)
            dsk_ref[...] = jnp.zeros_like(dsk_ref)

        valid = _attn_masks(i, nb)
        q = q_ref[...]
        dy_ = dy_ref[...]
        dly = dy_ * y_ref[...].astype(F32)
        lse_ = lse_ref[...]
        dqs = []
        for g in range(2):
            gs = slice(256 * g, 256 * g + 256)
            kcat = jnp.concatenate([kp[:, gs], ko[:, gs], kn[:, gs], kc[:, gs]], axis=0)
            vcat = jnp.concatenate([vp[:, gs], vo[:, gs], vn[:, gs], vc[:, gs]], axis=0)
            q4 = _stack_heads(q[:, gs])
            dy4 = _stack_heads(dy_[:, gs]).astype(BF16)
            lse4 = jnp.max(_stack_heads(lse_[:, gs], fill=-1e30), axis=-1, keepdims=True)
            delta = jnp.sum(_stack_heads(dly[:, gs]), axis=-1, keepdims=True)
            sink = _per_head_rows([sk_ref[4 * g + j] for j in range(4)])
            pr = jnp.where(valid, jnp.exp(_dot(q4, kcat, NT) - lse4), 0.0)
            dsb = (pr * (_dot(dy4, vcat, NT) - delta)).astype(BF16)
            dsink = jnp.exp(sink - lse4) * delta
            for j in range(4):
                dsk_ref[4 * g + j:4 * g + j + 1, :] += jnp.broadcast_to(
                    -jnp.sum(dsink[j * BLK:(j + 1) * BLK], axis=0, keepdims=True), (1, 128))
            dqs.append(_unstack_heads(_dot(dsb, kcat)))
            dkg = _dot(dsb, q4, TN)
            dvg = _dot(pr.astype(BF16), dy4, TN)
            dkw_ref[0, :, gs] = dkg[:3 * BLK]
            dvw_ref[0, :, gs] = dvg[:3 * BLK]
            dkc_ref[:, gs] += dkg[3 * BLK:]
            dvc_ref[:, gs] += dvg[3 * BLK:]
        dq_ref[...] = jnp.concatenate(dqs, axis=1)

    blk, ctx = _attn_specs(nb)
    out = pl.BlockSpec((BLK, ATW), lambda i: (i, 0))
    win = pl.BlockSpec((1, 3 * BLK, ATW), lambda i: (i, 0, 0))
    acc = _full((L, ATW))
    return _pcall(body, name="attn_bwd", grid=(nb,),
                  in_specs=[pl.BlockSpec(memory_space=pltpu.SMEM), blk(0),
                            blk(-1), blk(0), blk(1), ctx, blk(-1), blk(0), blk(1), ctx, out, out, out],
                  out_specs=[out, win, win, acc, acc, _full((8, 128))],
                  out_shape=[jax.ShapeDtypeStruct((s_len, ATW), F32),
                             jax.ShapeDtypeStruct((nb, 3 * BLK, ATW), F32),
                             jax.ShapeDtypeStruct((nb, 3 * BLK, ATW), F32),
                             jax.ShapeDtypeStruct((L, ATW), F32), jax.ShapeDtypeStruct((L, ATW), F32),
                             jax.ShapeDtypeStruct((8, 128), F32)], carry=carry)(
                      sinks, qr, k4, k4, k4, k4, v4, v4, v4, v4, y, lse, dy)


def _attn_post(p, cos, sin, qnw8, knw2, bd512, bd128, dupt, dq, dkw, dvw, dkc, dvc, dp, carry=None):
    tt = p.shape[0]
    s_len = tt - L
    nb = s_len // BLK
    nctx = L // BLK

    def body(q_ref, kv_ref, cos_ref, sin_ref, qw_ref, kw_ref, b5_ref, b1_ref, dupt_ref,
             dq_ref, kwp, kwo, kwn, vwp, vwo, vwn, dkc_ref, dvc_ref, _dp_in,
             dp_ref, dqw_ref, dkw_ref):
        t = pl.program_id(0)
        j = t - nctx

        @pl.when(t == 0)
        def _():
            dqw_ref[...] = jnp.zeros_like(dqw_ref)
            dkw_ref[...] = jnp.zeros_like(dkw_ref)

        is_lat = t >= nctx
        cos_, sin_ = cos_ref[...], sin_ref[...]
        has_p = is_lat & (j >= 1)
        has_n = is_lat & (j <= nb - 2)
        dk4 = (jnp.where(is_lat, kwo[0], dkc_ref[...]) + jnp.where(has_p, kwp[0], 0.0)
               + jnp.where(has_n, kwn[0], 0.0))
        dv4 = (jnp.where(is_lat, vwo[0], dvc_ref[...]) + jnp.where(has_p, vwp[0], 0.0)
               + jnp.where(has_n, vwn[0], 0.0))
        dkr = _dot(dk4, dupt_ref[...], prec=HI)
        dv = _dot(dv4, dupt_ref[...], prec=HI)
        kv = kv_ref[...]
        k = kv[:, :128]
        kw = kw_ref[...]
        rk = lax.rsqrt(_head_mean(k * k, b1_ref[...]) + EPS)
        xk = k * rk
        dkn = dkr * cos_ + _rot(dkr * sin_)
        dxk = dkn * kw
        dk = rk * (dxk - xk * _head_mean(dxk * xk, b1_ref[...]))
        dkw_ref[...] += jnp.sum(dkn * xk, axis=0, keepdims=True)
        q = q_ref[...]
        qw = qw_ref[...]
        rq = lax.rsqrt(_head_mean(q * q, b5_ref[...]) + EPS)
        xq = q * rq
        cos4 = jnp.concatenate([cos_] * 4, axis=1)
        sin4 = jnp.concatenate([sin_] * 4, axis=1)
        dqr = jnp.where(is_lat, dq_ref[...], 0.0) * (HDIM ** -0.5)
        dqn = dqr * cos4 + _rot(dqr * sin4)
        dxq = dqn * qw
        dqraw = rq * (dxq - xq * _head_mean(dxq * xq, b5_ref[...]))
        dqw_ref[...] += jnp.sum(dqn * xq, axis=0, keepdims=True)
        dp_ref[...] = jnp.concatenate([dqraw, dk, dv], axis=1).astype(BF16)

    row = lambda w, cb: pl.BlockSpec((BLK, w), lambda t: (t, cb))
    lat = pl.BlockSpec((BLK, ATW), lambda t: (jnp.maximum(t - nctx, 0), 0))

    def part(off):
        return pl.BlockSpec((1, BLK, ATW), lambda t: (jnp.clip(t - nctx + off, 0, nb - 1), 1 - off, 0))

    cacc = pl.BlockSpec((BLK, ATW), lambda t: (jnp.minimum(t, nctx - 1), 0))
    return _pcall(body, name="attn_post", grid=(tt // BLK,),
                  in_specs=[row(ATW, C_QRAW), row(256, C_KV), row(128, 0), row(128, 0),
                            _full((1, ATW)), _full((1, 128)), _full((ATW, ATW)), _full((128, 128)),
                            _full((ATW, 128)), lat, part(-1), part(0), part(1), part(-1), part(0), part(1),
                            cacc, cacc, ANY],
                  out_specs=[pl.BlockSpec((BLK, 768), lambda t: (t, C_QKV)), _full((1, ATW)), _full((1, 128))],
                  out_shape=[jax.ShapeDtypeStruct(dp.shape, BF16), jax.ShapeDtypeStruct((1, ATW), F32),
                             jax.ShapeDtypeStruct((1, 128), F32)],
                  aliases={18: 0}, carry=carry)(p, p, cos, sin, qnw8, knw2, bd512, bd128, dupt,
                                   dq, dkw, dkw, dkw, dvw, dvw, dvw, dkc, dvc, dp)


def _branch_merge(y_hg, y_at, bh4, ba4, p):
    s_len = y_hg.shape[0]

    def body(yh_ref, ya_ref, bh_ref, ba_ref, gh_ref, ga_ref, ah_ref, aa_ref, m_ref):
        yh, ya = yh_ref[...], ya_ref[...]
        ah = jnp.concatenate([_bdot(yh, bh_ref[j]) for j in range(4)], axis=1)
        aa = jnp.concatenate([_bdot(ya, ba_ref[j]) for j in range(4)], axis=1)
        ah_ref[...] = ah.astype(BF16)
        aa_ref[...] = aa.astype(BF16)
        m_ref[...] = (_sig(gh_ref[...]) * ah + _sig(ga_ref[...]) * aa).astype(BF16)

    row = pl.BlockSpec((TM, D), lambda i: (i, 0))
    y = pl.BlockSpec((TM, HGW), lambda i: (i, 0))
    f = jax.ShapeDtypeStruct((s_len, D), BF16)
    return _pcall(body, name="branch_merge", grid=(s_len // TM,),
                  in_specs=[y, y, _full(bh4.shape), _full(ba4.shape),
                            pl.BlockSpec((TM, D), lambda i: (i + 1, 2)), pl.BlockSpec((TM, D), lambda i: (i + 1, 3))],
                  out_specs=[row, row, row],
                  out_shape=[f, f, jax.ShapeDtypeStruct((s_len, D), BF16)])(y_hg, y_at, bh4, ba4, p, p)


def _branch_bwd(dmh, dma, bh4, ba4, y_hg, y_at):
    s_len = dmh.shape[0]
    nk = s_len // TS
    ns = D // 4

    def body(dh_ref, da_ref, bh_ref, ba_ref, yh_ref, ya_ref, dyh_ref, dya_ref, gh_ref, ga_ref, acc_h, acc_a):
        t = pl.program_id(0)

        @pl.when(t == 0)
        def _():
            acc_h[...] = jnp.zeros_like(acc_h)
            acc_a[...] = jnp.zeros_like(acc_a)

        for d_ref, w_ref, y_ref, dy_ref, acc in ((dh_ref, bh_ref, yh_ref, dyh_ref, acc_h),
                                                 (da_ref, ba_ref, ya_ref, dya_ref, acc_a)):
            y = y_ref[...]
            dy = jnp.zeros((TS, HGW), F32)
            for j in range(4):
                dj = d_ref[:, j * ns:(j + 1) * ns]
                dy = dy + _bdot(dj, w_ref[j], NT)
                acc[j] += _bdot(y, dj, TN)
            dy_ref[...] = dy

        @pl.when(t == nk - 1)
        def _():
            gh_ref[...] = acc_h[...].astype(BF16)
            ga_ref[...] = acc_a[...].astype(BF16)

    dm = pl.BlockSpec((TS, D), lambda t: (t, 0))
    y = pl.BlockSpec((TS, HGW), lambda t: (t, 0))
    w = _full(bh4.shape)
    fy = jax.ShapeDtypeStruct((s_len, HGW), F32)
    gw = jax.ShapeDtypeStruct(bh4.shape, BF16)
    return _pcall(body, name="branch_bwd", grid=(nk,), in_specs=[dm, dm, w, w, y, y],
                  out_specs=[y, y, w, w], out_shape=[fy, fy, gw, gw],
                  scratch=[pltpu.VMEM(bh4.shape, F32)] * 2)(dmh, dma, bh4, ba4, y_hg, y_at)


def _merge_bwd(dattn, w_o, mixed, ah, aa, p, carry=None):
    tt = p.shape[0]
    s_len = tt - L
    nt = tt // TM

    def body(da_ref, wo_ref, mx_ref, ah_ref, aa_ref, gh_ref, ga_ref, dp_ref, dmh_ref, dma_ref, go_ref, acc):
        i = pl.program_id(0)

        @pl.when(i == 0)
        def _():
            dp_ref[...] = jnp.zeros_like(dp_ref)
            acc[...] = jnp.zeros_like(acc)

        @pl.when(i >= 1)
        def _():
            da = da_ref[...]
            acc[...] += _bdot(mx_ref[...], da, TN)
            dm_ = _bdot(da, wo_ref[...], NT)
            sh, sa = _sig(gh_ref[...]), _sig(ga_ref[...])
            dp_ref[...] = jnp.concatenate([dm_ * ah_ref[...].astype(F32) * sh * (1.0 - sh),
                                           dm_ * aa_ref[...].astype(F32) * sa * (1.0 - sa)], axis=1).astype(BF16)
            dmh_ref[...] = (dm_ * sh).astype(BF16)
            dma_ref[...] = (dm_ * sa).astype(BF16)

        @pl.when(i == nt - 1)
        def _():
            go_ref[...] = acc[...].astype(BF16)

    lat = pl.BlockSpec((TM, D), lambda i: (jnp.maximum(i - 1, 0), 0))
    return _pcall(body, name="merge_bwd", grid=(nt,),
                  in_specs=[lat, _full((D, D)), lat, lat, lat, pl.BlockSpec((TM, D), lambda i: (i, 2)),
                            pl.BlockSpec((TM, D), lambda i: (i, 3))],
                  out_specs=[pl.BlockSpec((TM, 2 * D), lambda i: (i, C_GATES)), lat, lat, _full((D, D))],
                  out_shape=[jax.ShapeDtypeStruct((tt, NCOL), BF16), jax.ShapeDtypeStruct((s_len, D), BF16),
                             jax.ShapeDtypeStruct((s_len, D), BF16), jax.ShapeDtypeStruct((D, D), BF16)],
                  scratch=[pltpu.VMEM((D, D), F32)], carry=carry)(dattn, w_o, mixed, ah, aa, p, p)


def _local_step(x, ctx, tgt, mod, modc, nw1, nw2, lg, hw, qnw, knw, sinks,
                w_in, wts, dist=None):
    s_len = x.shape[0]
    tt = s_len + L
    ss1 = jnp.stack([modc, mod[0:2]])
    ss2 = mod[3:5][None]
    g1, g2 = mod[2:3], mod[5:6]
    hw4 = jnp.tile(hw, (1, 4))
    qnw8 = jnp.tile(qnw, (1, 8))
    knw2 = jnp.tile(knw, (1, 2))
    cos, sin = _rope_tables(s_len)
    bd512, bd128 = _blockdiag(ATW, HDIM), _blockdiag(128, HDIM)
    dupm = _dup_matrix()
    dup, dupt = jnp.asarray(dupm, BF16), jnp.asarray(dupm.T, F32)
    tmt = tt

    def four(b):
        return b.reshape(4, 2 * b.shape[1], b.shape[2])

    def halves(g):
        return g.reshape(4, 2, g.shape[1] // 2, g.shape[2])

    h = _mod1(ctx, x, nw1, ss1)
    if dist is None:
        bh4, ba4, w_o, g4, u4, dn4 = wts
        p = _mm_in(h, w_in, tmt)
        o0, st0 = _hgrn_fwd(p, lg, rev=False)
        o1, st1, y_hg = _hgrn_fwd(p, lg, rev=True, readout=(o0, hw4))
    else:
        core, chip = dist
        half = wts[3].shape[1] // 2
        p, (o8, g8a) = _mm_in(h, w_in, tmt, carry=_carry_gather([wts[2], wts[3]], rows=[None, (0, half)]))
        (o0, st0), (g8,) = _hgrn_fwd(p, lg, rev=False, carry=_carry_gather([g8a], rows=[(half, half)]))
        (o1, st1, y_hg), (dn8a, bh8, ba8) = _hgrn_fwd(
            p, lg, rev=True, readout=(o0, hw4),
            carry=_carry_gather([wts[5], wts[0], wts[1]], rows=[(0, half), None, None]))
        bh4, ba4, w_o, g4 = four(bh8), four(ba8), four(o8).reshape(D, D), four(g8)
    qr, k4, v4 = _qk_prep(p, cos, sin, qnw8, knw2, bd512, bd128, dup)
    if dist is None:
        y_at, lse = _attn_fwd(qr, k4, v4, sinks)
    else:
        (y_at, lse), (u8, dn8) = _attn_fwd(qr, k4, v4, sinks,
                                           carry=_carry_gather([wts[4], dn8a], rows=[None, (half, half)]))
        u4, dn4 = four(u8), four(dn8)
    ah, aa, mixed = _branch_merge(y_hg, y_at, bh4, ba4, p)
    ao, x1, h2 = _out_proj_mod2(mixed, w_o, x, g1, nw2, ss2)
    a4, b4, z4 = _ffn_up(h2, g4, u4)
    sq, dx2, dyb, dg2 = _ffn_down_loss(z4, dn4, x1, g2, tgt)

    da4, db4 = _ffn_dz(dyb, dn4, a4, b4)
    g_dn = _ffn_gdn(z4, dyb)
    if dist is None:
        dh2 = _ffn_dh2(da4, db4, g4, u4)
    else:
        dn_units = [halves(g_dn)]
        dh2, dn_recv = _ffn_dh2(da4, db4, g4, u4, carry=_carry_pairx(dn_units))
        dn_pairs = _rs_pair_add(dn_units, dn_recv, core)
    if dist is None:
        g_g, g_u = _ffn_ggu(h2, da4, db4)
    else:
        (g_g, g_u), c_dn = _ffn_ggu(h2, da4, db4, carry=_carry_chipx(dn_pairs))
        red_dn = _rs_chip_add(dn_pairs, c_dn, core, chip)
    dx1, dattn, dss2, dnw2, dg1 = _mod2_bwd(x1, dh2, dx2, ao, nw2, ss2, g1)
    if dist is None:
        dp, dmh, dma, g_o = _merge_bwd(dattn, w_o, mixed, ah, aa, p)
    else:
        gu_units = [halves(g_g), halves(g_u)]
        (dp, dmh, dma, g_o), gu_recv = _merge_bwd(dattn, w_o, mixed, ah, aa, p, carry=_carry_pairx(gu_units))
        ffn_pairs = list(dn_pairs) + list(_rs_pair_add(gu_units, gu_recv, core))
    dy_hg, dy_at, g_bh, g_ba = _branch_bwd(dmh, dma, bh4, ba4, y_hg, y_at)
    if dist is None:
        dp, do, dhw4 = _readout_bwd(o0, o1, p, hw4, dy_hg, dp)
        dq, dkw, dvw, dkc, dvc, dsk = _attn_bwd(qr, k4, v4, sinks, y_at, lse, dy_at)
        dp, dqnw8, dknw2 = _attn_post(p, cos, sin, qnw8, knw2, bd512, bd128, dupt, dq, dkw, dvw, dkc, dvc, dp)
    else:
        mix_units = [halves(g_bh), halves(g_ba), halves(g_o.reshape(4, D // 4, D))]
        (dp, do, dhw4), mix_recv = _readout_bwd(o0, o1, p, hw4, dy_hg, dp, carry=_carry_pairx(mix_units))
        mix_pairs = _rs_pair_add(mix_units, mix_recv, core)
        (dq, dkw, dvw, dkc, dvc, dsk), bwd = _attn_bwd(
            qr, k4, v4, sinks, y_at, lse, dy_at,
            carry=_carry_join(_carry_chipx(ffn_pairs[1:2]), _carry_sibx(red_dn)))
        red_g = _rs_chip_add(ffn_pairs[1:2], bwd[0:1], core, chip)
        (dp, dqnw8, dknw2), post = _attn_post(
            p, cos, sin, qnw8, knw2, bd512, bd128, dupt, dq, dkw, dvw, dkc, dvc, dp, carry=_carry_sibx(red_g))
    if dist is None:
        dp, dv0, dq0, dlg0 = _hgrn_bwd(p, lg, do, st0, dp, None, rev=False)
        dp, dlg1 = _hgrn_bwd(p, lg, do, st1, dp, (dv0, dq0), rev=True)
    else:
        (dp, dv0, dq0, dlg0), c_u = _hgrn_bwd(p, lg, do, st0, dp, None, rev=False,
                                              carry=_carry_chipx(ffn_pairs[2:3]))
        red_u = _rs_chip_add(ffn_pairs[2:3], c_u, core, chip)
        (dp, dlg1), last = _hgrn_bwd(p, lg, do, st1, dp, (dv0, dq0), rev=True,
                                     carry=_carry_join(_carry_chipx(mix_pairs), _carry_sibx(red_u)))
        mix_reds = _rs_chip_add(mix_pairs, last[0:3], core, chip)
        ffn_done = bwd[1:2] + post[0:1] + last[3:4]
    g_in = _mm_gin(dp, h, tmt)
    if dist is None:
        dh = _mm_dh(dp, w_in, tmt)
        gx, dss1, dnw1 = _mod1_bwd(ctx, x, dh, dx1, nw1, ss1)
        rs = None
    else:
        in_units = [halves(g_in.reshape(4, NCOL // 4, D))]
        in_pairs = _rs_pair_add(in_units, _pair_exchange(in_units), core)
        first_rows, rest_rows = _split_rows(in_pairs[0].shape[1], IN_ROWS_WITH_DH)
        small_rows, rest_rows = _split_rows(rest_rows[1], IN_ROWS_WITH_SMALL, first=rest_rows[0])
        dh, both = _mm_dh(dp, w_in, tmt, carry=_carry_join(_carry_chipx(in_pairs, rows=first_rows),
                                                           _carry_sibx(mix_reds)))
        in_part, mix_done = both[0:1], both[1:4]
        gx, dss1, dnw1 = _mod1_bwd(ctx, x, dh, dx1, nw1, ss1)
        rs = dict(ffn_done=ffn_done, mix_done=mix_done, in_pairs=in_pairs, in_part=in_part, small_rows=small_rows,
                  rest_rows=rest_rows)

    dmod = jnp.concatenate([dss1[1], dg1, dss2, dg2], axis=0)
    dmodc = dss1[0]
    raw = (dss1, dg1, dss2, dg2, dnw1, dnw2, dhw4, dqnw8, dknw2, dsk, dlg0, dlg1)
    small = dict(raw=raw, dmod=dmod, dmodc=dmodc, dnw1=dnw1, dnw2=dnw2,
                 dhw=dhw4.reshape(4, HGD).sum(0, keepdims=True),
                 dqnw=dqnw8.reshape(8, HDIM).sum(0, keepdims=True),
                 dknw=dknw2.reshape(2, HDIM).sum(0, keepdims=True),
                 dsinks=dsk[:, 0], dlg=jnp.concatenate([dlg0, dlg1], axis=0))
    big = dict(w_in=g_in, w_bh=g_bh, w_ba=g_ba, w_o=g_o, w_g=g_g, w_u=g_u, w_dn=g_dn)
    return sq, gx, big, small, rs


def _place():
    x, y, c = lax.axis_index("x"), lax.axis_index("y"), lax.axis_index("c")
    return x, y, c


def _gather_blocks(x_refs, out_refs, send_sems, recv_sems, local_sems):
    n = len(out_refs)
    x, y, c = _place()
    me, sibling = (x, y, c), (x, y, 1 - c)
    chips = [(1 - x, y), (x, 1 - y), (1 - x, 1 - y)]

    def slot(u, px, py, pc):
        return out_refs[u].at[4 * px + 2 * py + pc]

    def copy(u, k, block, to, src=None):
        return pltpu.make_async_remote_copy(
            src_ref=slot(u, *block) if src is None else src, dst_ref=slot(u, *block),
            send_sem=send_sems.at[u, k], recv_sem=recv_sems.at[u, k], device_id=to, device_id_type=MESH)

    mines = [pltpu.make_async_copy(x_refs[u], slot(u, *me), local_sems.at[u]) for u in range(n)]
    for cp in mines:
        cp.start()
    first = []
    for u in range(n):
        first.append(copy(u, 0, me, sibling, src=x_refs[u]))
        first += [copy(u, 1 + j, me, (*chip, c), src=x_refs[u]) for j, chip in enumerate(chips)]
    for cp in first:
        cp.start()
    passed = []
    for j, chip in enumerate(chips):
        for u in range(n):
            copy(u, 1 + j, (*chip, c), me).wait_recv()
            fwd = copy(u, 4 + j, (*chip, c), sibling)
            fwd.start()
            passed.append(fwd)
    for u in range(n):
        copy(u, 0, sibling, me).wait_recv()
    for j, chip in enumerate(chips):
        for u in range(n):
            copy(u, 4 + j, (*chip, 1 - c), me).wait_recv()
    for cp in first + passed:
        cp.wait_send()
    for cp in mines:
        cp.wait()


def _gather_sems(n):
    return [pltpu.SemaphoreType.DMA((n, 7)), pltpu.SemaphoreType.DMA((n, 7)), pltpu.SemaphoreType.DMA((n,))]


def _cast_place(ws, c, dev):
    n = len(ws)

    def body(s_ref, *refs):
        for u in range(n):
            refs[n + u][0] = refs[u][...].astype(BF16)

    in_specs, out_specs, out_shape = [], [], []
    for w in ws:
        q, cols = w.shape[0] // 4, w.shape[1]
        in_specs.append(pl.BlockSpec((q, cols), lambda i, s: (2 * s[0] + i, 0)))
        out_specs.append(pl.BlockSpec((1, q, cols), lambda i, s: (s[1], i, 0)))
        out_shape.append(jax.ShapeDtypeStruct((8, 2 * q, cols), BF16))
    return pl.pallas_call(
        body, name="cast_place",
        grid_spec=pltpu.PrefetchScalarGridSpec(num_scalar_prefetch=1, grid=(2,), in_specs=in_specs,
                                               out_specs=out_specs),
        out_shape=_out_hbm(out_shape),
        compiler_params=pltpu.CompilerParams(vmem_limit_bytes=48 << 20))(jnp.stack([c, dev]), *_in_hbm(ws))


def _gather_phases(out_refs, send_sems, recv_sems, rows=None):
    n = len(out_refs)
    x, y, c = _place()
    me, sibling = (x, y, c), (x, y, 1 - c)
    chips = [(1 - x, y), (x, 1 - y), (1 - x, 1 - y)]

    def copy(u, k, block, to):
        px, py, pc = block
        ref = out_refs[u].at[4 * px + 2 * py + pc]
        if rows is not None and rows[u] is not None:
            ref = ref.at[pl.ds(rows[u][0], rows[u][1])]
        return pltpu.make_async_remote_copy(src_ref=ref, dst_ref=ref, send_sem=send_sems.at[u, k],
                                            recv_sem=recv_sems.at[u, k], device_id=to, device_id_type=MESH)

    def start():
        for u in range(n):
            copy(u, 0, me, sibling).start()
            for j, chip in enumerate(chips):
                copy(u, 1 + j, me, (*chip, c)).start()

    def mid():
        for j, chip in enumerate(chips):
            for u in range(n):
                copy(u, 1 + j, (*chip, c), me).wait_recv()
                copy(u, 4 + j, (*chip, c), sibling).start()

    def end():
        for u in range(n):
            copy(u, 0, sibling, me).wait_recv()
        for j, chip in enumerate(chips):
            for u in range(n):
                copy(u, 4 + j, (*chip, 1 - c), me).wait_recv()
        for u in range(n):
            copy(u, 0, me, sibling).wait_send()
            for j, chip in enumerate(chips):
                copy(u, 1 + j, me, (*chip, c)).wait_send()
                copy(u, 4 + j, (*chip, c), sibling).wait_send()

    return start, mid, end


def _carry_gather(bufs, rows=None):
    n = len(bufs)
    return _Carry(bufs, [jax.ShapeDtypeStruct(b.shape, b.dtype) for b in bufs], {u: u for u in range(n)},
                  [pltpu.SemaphoreType.DMA((n, 7)), pltpu.SemaphoreType.DMA((n, 7))],
                  lambda ins, outs, sems: _gather_phases(outs, *sems, rows=rows), "both")


def _ag_small(raw, sq, pairs, rows, into):
    def body(dss1, dg1, dss2, dg2, dnw1, dnw2, dhw4, dqnw8, dknw2, dsk, dlg0, dlg1, sq_ref, p_ref, _into,
             out_ref, tot_ref, r_ref, blk, send_sems, recv_sems, chip_send, chip_recv):
        _peer_barrier("both")
        x, y, c = _place()
        start, mid, end = _gather_phases([out_ref], send_sems, recv_sems)
        chip_start, _, chip_end = _chipx_phases([p_ref], [r_ref], chip_send, chip_recv, rows=rows)
        blk[...] = jnp.zeros_like(blk)
        blk[0:2, :] = dss1[1]
        blk[2:3, :] = dg1[...]
        blk[3:5, :] = dss2[...]
        blk[5:6, :] = dg2[...]
        blk[6:8, :] = dss1[0]
        blk[8:9, :] = dnw1[...]
        blk[9:10, :] = dnw2[...]
        blk[10:11, 0:HGW] = dhw4[...]
        blk[10:11, HGW:D] = dqnw8[...]
        blk[11:12, 0:128] = dknw2[...]
        blk[12:14, 0:HGW] = dlg0[0]
        blk[14:16, 0:HGW] = dlg1[0]
        blk[16:24, 0:128] = dsk[...]
        blk[24:25, :] = sq_ref[...]
        out_ref[4 * x + 2 * y + c] = blk[...]
        start()
        chip_start()
        mid()
        end()
        acc = out_ref[0]
        for i in range(1, 8):
            acc = acc + out_ref[i]
        tot_ref[...] = acc
        chip_end()

    vm = pl.BlockSpec(memory_space=pltpu.VMEM)
    g2, tot, part = pl.pallas_call(
        body, name="ag_small",
        out_shape=[jax.ShapeDtypeStruct((8, 32, D), F32), jax.ShapeDtypeStruct((32, D), F32),
                   jax.ShapeDtypeStruct(into.shape, into.dtype)],
        in_specs=[vm] * 13 + [ANY, ANY], out_specs=[vm, vm, ANY], input_output_aliases={14: 2},
        scratch_shapes=[pltpu.VMEM((32, D), F32)] + [pltpu.SemaphoreType.DMA((1, 7))] * 2
        + [pltpu.SemaphoreType.DMA((1, 3))] * 2,
        compiler_params=pltpu.CompilerParams(collective_id=BARRIER_IDS["both"]))(*raw, sq, *_in_hbm([pairs[0], into]))
    return (g2, tot), part


def _pairx_shapes(units):
    return [jax.ShapeDtypeStruct((4,) + g.shape[2:], g.dtype) for g in units]


def _pairx_phases(g_refs, r_refs, send_sems, recv_sems):
    n = len(g_refs)
    x, y, c = _place()
    cps = [pltpu.make_async_remote_copy(
        src_ref=g_refs[u].at[j, 1 - c], dst_ref=r_refs[u].at[j], send_sem=send_sems.at[u, j],
        recv_sem=recv_sems.at[u, j], device_id=(x, y, 1 - c), device_id_type=MESH)
        for u in range(n) for j in range(4)]

    def start():
        for cp in cps:
            cp.start()

    def end():
        for cp in cps:
            cp.wait()

    return start, None, end


def _carry_pairx(units):
    n = len(units)
    return _Carry(units, _pairx_shapes(units), {},
                  [pltpu.SemaphoreType.DMA((n, 4)), pltpu.SemaphoreType.DMA((n, 4))],
                  lambda ins, outs, sems: _pairx_phases(ins, outs, *sems), "sib")


def _pair_exchange(units):
    n = len(units)

    def body(*refs):
        _peer_barrier("sib")
        start, _, end = _pairx_phases(refs[:n], refs[n:2 * n], *refs[2 * n:])
        start()
        end()

    return pl.pallas_call(
        body, name="pair_exchange", out_shape=_pairx_shapes(units), in_specs=[ANY] * n, out_specs=[ANY] * n,
        scratch_shapes=[pltpu.SemaphoreType.DMA((n, 4))] * 2,
        compiler_params=pltpu.CompilerParams(collective_id=BARRIER_IDS["sib"]))(*_in_hbm(units))


IN_ROWS_WITH_DH = 0.55


IN_ROWS_WITH_SMALL = 0.45


def _split_rows(h, share, first=0):
    k = int(h * share) // BF16_SUBLANES * BF16_SUBLANES
    return (first, k), (first + k, h - k)


def _rs_pair_add(units, recvs, c):
    n = len(units)

    def body(c_ref, *refs):
        for u in range(n):
            refs[2 * n + u][...] = (refs[u][0].astype(F32) + refs[n + u][...].astype(F32)).astype(BF16)

    in_specs, out_specs, out_shape = [], [], []
    for g in units:
        h, w = g.shape[2:]
        in_specs.append(pl.BlockSpec((1, 1, h, w), lambda j, cr: (j, cr[0], 0, 0)))
    for g in units:
        h, w = g.shape[2:]
        in_specs.append(pl.BlockSpec((1, h, w), lambda j, cr: (j, 0, 0)))
        out_specs.append(pl.BlockSpec((1, h, w), lambda j, cr: (j, 0, 0)))
        out_shape.append(jax.ShapeDtypeStruct((4, h, w), BF16))
    return pl.pallas_call(
        body, name="rs_pair_add",
        grid_spec=pltpu.PrefetchScalarGridSpec(num_scalar_prefetch=1, grid=(4,), in_specs=in_specs,
                                               out_specs=out_specs),
        out_shape=_out_hbm(out_shape),
        compiler_params=pltpu.CompilerParams(vmem_limit_bytes=48 << 20))(
            c.reshape(1), *_in_hbm(list(units) + list(recvs)))


def _chipx_phases(p_refs, r_refs, send_sems, recv_sems, rows=None):
    n = len(p_refs)
    x, y, c = _place()
    k = 2 * x + y

    def part(ref):
        return ref if rows is None else ref.at[pl.ds(rows[0], rows[1])]

    sends = []
    for d in range(1, 4):
        j = (k + d) % 4
        for u in range(n):
            sends.append(pltpu.make_async_remote_copy(
                src_ref=part(p_refs[u].at[j]), dst_ref=part(r_refs[u].at[k]), send_sem=send_sems.at[u, d - 1],
                recv_sem=recv_sems.at[u, d - 1], device_id=(j // 2, j % 2, c), device_id_type=MESH))

    def start():
        for cp in sends:
            cp.start()

    def end():
        for d in range(1, 4):
            src = (k + 4 - d) % 4
            for u in range(n):
                pltpu.make_async_remote_copy(
                    src_ref=part(p_refs[u].at[src]), dst_ref=part(r_refs[u].at[src]),
                    send_sem=send_sems.at[u, d - 1], recv_sem=recv_sems.at[u, d - 1], device_id=(x, y, c),
                    device_id_type=MESH).wait_recv()
        for cp in sends:
            cp.wait_send()

    return start, None, end


def _carry_chipx(pairs, rows=None, into=None):
    n = len(pairs)
    sems = [pltpu.SemaphoreType.DMA((n, 3)), pltpu.SemaphoreType.DMA((n, 3))]
    shapes = [jax.ShapeDtypeStruct(p.shape, p.dtype) for p in pairs]
    if into is None:
        return _Carry(pairs, shapes, {}, sems, lambda ins, outs, s: _chipx_phases(ins, outs, *s, rows=rows),
                      "chips")
    return _Carry(list(pairs) + list(into), shapes, {n + u: u for u in range(n)}, sems,
                  lambda ins, outs, s: _chipx_phases(ins[:n], outs, *s, rows=rows), "chips")


def _rs_chip_add(pairs, contribs, c, chip):
    n = len(pairs)

    def body(s_ref, *refs):
        for u in range(n):
            a, b, c_, d = refs[4 * u:4 * u + 4]
            refs[4 * n + u][0] = ((a[0].astype(F32) + b[0].astype(F32)) + c_[0].astype(F32)) + d[0].astype(F32)

    in_specs, out_specs, out_shape, args = [], [], [], []
    for p, r in zip(pairs, contribs):
        h, w = p.shape[1] // 2, p.shape[2]
        in_specs += [pl.BlockSpec((1, h, w), functools.partial(lambda d, i, s: ((s[1] + d) % 4, i, 0), d))
                     for d in range(4)]
        args += [p, r, r, r]
        out_specs.append(pl.BlockSpec((1, h, w), lambda i, s: (s[0], i, 0)))
        out_shape.append(jax.ShapeDtypeStruct((2, 2 * h, w), F32))
    return pl.pallas_call(
        body, name="rs_chip_add",
        grid_spec=pltpu.PrefetchScalarGridSpec(num_scalar_prefetch=1, grid=(2,), in_specs=in_specs,
                                               out_specs=out_specs),
        out_shape=_out_hbm(out_shape),
        compiler_params=pltpu.CompilerParams(vmem_limit_bytes=48 << 20))(jnp.stack([c, chip]), *_in_hbm(args))


def _rs_sibling_gather(reds, blks):
    n, k = len(reds), len(blks)
    vm = pl.BlockSpec(memory_space=pltpu.VMEM)

    def body(*refs):
        blk_in, red_out, blk_out = refs[n:n + k], refs[n + k:2 * n + k], refs[2 * n + k:2 * (n + k)]
        sems = refs[2 * (n + k):]
        _peer_barrier("both")
        start, _, end = _sibx_phases(red_out, *sems[:2])
        start()
        _gather_blocks(blk_in, blk_out, *sems[2:])
        end()

    res = pl.pallas_call(
        body, name="rs_sibling_gather",
        out_shape=[jax.ShapeDtypeStruct(r.shape, r.dtype) for r in reds]
        + [jax.ShapeDtypeStruct((8,) + b.shape, b.dtype) for b in blks],
        in_specs=[ANY] * n + [vm] * k, out_specs=[ANY] * n + [vm] * k,
        input_output_aliases={u: u for u in range(n)},
        scratch_shapes=[pltpu.SemaphoreType.DMA((n,))] * 2 + _gather_sems(k),
        compiler_params=pltpu.CompilerParams(collective_id=BARRIER_IDS["both"]))(*_in_hbm(reds), *blks)
    return res[:n], res[n:]


def _sibx_phases(o_refs, send_sems, recv_sems):
    n = len(o_refs)
    x, y, c = _place()
    cps = [pltpu.make_async_remote_copy(
        src_ref=o_refs[u].at[c], dst_ref=o_refs[u].at[c], send_sem=send_sems.at[u], recv_sem=recv_sems.at[u],
        device_id=(x, y, 1 - c), device_id_type=MESH) for u in range(n)]

    def start():
        for cp in cps:
            cp.start()

    def end():
        for u in range(n):
            cps[u].wait_send()
            pltpu.make_async_remote_copy(
                src_ref=o_refs[u].at[1 - c], dst_ref=o_refs[u].at[1 - c], send_sem=send_sems.at[u],
                recv_sem=recv_sems.at[u], device_id=(x, y, 1 - c), device_id_type=MESH).wait_recv()

    return start, None, end


def _carry_sibx(reds):
    n = len(reds)
    return _Carry(reds, [jax.ShapeDtypeStruct(r.shape, r.dtype) for r in reds], {u: u for u in range(n)},
                  [pltpu.SemaphoreType.DMA((n,))] * 2, lambda ins, outs, sems: _sibx_phases(outs, *sems), "sib")


def _prologue(blk, c_ctx, w, b, in8):
    n = w.shape[1]

    def body(blk_ref, cctx_ref, w_ref, b_ref, _in_in, g0_ref, c16_ref, g1_ref, in_ref, s1, r1, l1, s2, r2, s3, r3):
        _peer_barrier("both")
        x, y, c = _place()
        start, mid, end = _gather_phases([in_ref], s3, r3)
        start_mod, mid_mod, end_mod = _gather_phases([g1_ref], s2, r2)
        _gather_blocks([blk_ref], [g0_ref], s1, r1, l1)
        start()
        c16 = jnp.concatenate([g0_ref[i, 0:1, :] for i in range(8)] + [cctx_ref[...], jnp.zeros((7, D), F32)],
                              axis=0)
        c16_ref[...] = c16
        g1_ref[4 * x + 2 * y + c] = _dot(c16 * _sig(c16), w_ref[...], prec=HI) + b_ref[...]
        start_mod()
        mid()
        mid_mod()
        end()
        end_mod()

    vm = pl.BlockSpec(memory_space=pltpu.VMEM)
    return pl.pallas_call(
        body, name="prologue",
        out_shape=[jax.ShapeDtypeStruct((8, 8, D), F32), jax.ShapeDtypeStruct((16, D), F32),
                   jax.ShapeDtypeStruct((8, 16, n), F32), jax.ShapeDtypeStruct(in8.shape, in8.dtype)],
        in_specs=[vm, vm, vm, vm, ANY], out_specs=[vm, vm, vm, ANY], input_output_aliases={4: 3},
        scratch_shapes=_gather_sems(1) + [pltpu.SemaphoreType.DMA((1, 7))] * 4,
        compiler_params=pltpu.CompilerParams(vmem_limit_bytes=48 << 20,
                                             collective_id=BARRIER_IDS["both"]))(blk, c_ctx, w, b, in8)


def _ada_bwd(c16, dmod16, w, carry=None):
    n = w.shape[1]
    tn = 512

    def body(c_ref, d_ref, w_ref, gw_ref, gc_ref):
        j = pl.program_id(0)

        @pl.when(j == 0)
        def _():
            gc_ref[...] = jnp.zeros_like(gc_ref)

        cc = c_ref[...]
        dm = d_ref[...]
        gw_ref[...] = _dot(cc * _sig(cc), dm, TN, prec=HI)
        gc_ref[...] += _dot(dm, w_ref[...], NT, prec=HI)

    return _pcall(body, name="ada_bwd", grid=(n // tn,),
                  in_specs=[_full((16, D)), pl.BlockSpec((16, tn), lambda j: (0, j)),
                            pl.BlockSpec((D, tn), lambda j: (0, j))],
                  out_specs=[pl.BlockSpec((D, tn), lambda j: (0, j)), _full((16, D))],
                  out_shape=[jax.ShapeDtypeStruct((D, n), F32),
                             jax.ShapeDtypeStruct((16, D), F32)], carry=carry)(c16, dmod16, w)


def _adam_math(w, g, m, v):
    c1 = 1.0 - ADAM_B1 ** ADAM_STEP
    c2 = 1.0 - ADAM_B2 ** ADAM_STEP
    nm = ADAM_B1 * m + (1.0 - ADAM_B1) * g
    nv = ADAM_B2 * v + (1.0 - ADAM_B2) * (g * g)
    return -ADAM_LR * ((nm / c1) / (jnp.sqrt(nv / c2) + ADAM_EPS) + ADAM_WD * w), nm, nv


def _adamw_small(ws, gs, ms, vs):
    n = len(ws)

    def body(*refs):
        for u in range(n):
            d_, nm, nv = _adam_math(refs[u][...], refs[n + u][...], refs[2 * n + u][...], refs[3 * n + u][...])
            refs[4 * n + u][...] = d_
            refs[5 * n + u][...] = nm
            refs[6 * n + u][...] = nv

    specs = [_full(w.shape) for w in ws]
    shapes = [jax.ShapeDtypeStruct(w.shape, F32) for w in ws]
    out = _pcall(body, name="adamw_small", grid=(1,), in_specs=specs * 4, out_specs=specs * 3,
                 out_shape=shapes * 3)(*ws, *gs, *ms, *vs)
    return out[:n], out[n:2 * n], out[2 * n:]


def _cctx_grad(parts, c_ctx):
    def body(p_ref, c_ref, o_ref):
        acc = p_ref[0:1, :]
        for k in range(1, 4):
            acc = acc + p_ref[k:k + 1, :]
        cc = c_ref[...]
        s = _sig(cc)
        o_ref[...] = acc * (s * (1.0 + cc * (1.0 - s)))

    return _pcall(body, name="cctx_grad", grid=(1,), in_specs=[_full(parts.shape), _full((1, D))],
                  out_specs=_full((1, D)), out_shape=jax.ShapeDtypeStruct((1, D), F32))(parts, c_ctx)


ADAM_STEPS = 8


def _adamw_multi(ws, gs, ms, vs, *, name):
    n = len(ws)

    def body(*refs):
        for u in range(n):
            g = refs[n + u][...]
            refs[4 * n + u][...] = g
            refs[5 * n + u][...], refs[6 * n + u][...], refs[7 * n + u][...] = _adam_math(
                refs[u][...], g, refs[2 * n + u][...], refs[3 * n + u][...])

    specs = [pl.BlockSpec((w.shape[0] // ADAM_STEPS, w.shape[1]), lambda i: (i, 0)) for w in ws]
    shapes = [jax.ShapeDtypeStruct(w.shape, F32) for w in ws]
    out = _pcall(body, name=name, grid=(ADAM_STEPS,), in_specs=specs * 4, out_specs=specs * 4,
                 out_shape=shapes * 4)(*ws, *gs, *ms, *vs)
    return out[:n], out[n:2 * n], out[2 * n:3 * n], out[3 * n:]


def kernel(x, c, ctx, c_ctx, w_ada, b_ada, norm_mix_w, norm_ffn_w, w_in, hgrn_lb_logits, hgrn_norm_w, q_norm_w, k_norm_w, attn_sinks, w_branch_hgrn, w_branch_attn, w_out, w_ffn_gate, w_ffn_up, w_ffn_down, loss_target, m_c_ctx, m_w_ada, m_b_ada, m_norm_mix_w, m_norm_ffn_w, m_w_in, m_hgrn_lb_logits, m_hgrn_norm_w, m_q_norm_w, m_k_norm_w, m_attn_sinks, m_w_branch_hgrn, m_w_branch_attn, m_w_out, m_w_ffn_gate, m_w_ffn_up, m_w_ffn_down, v_c_ctx, v_w_ada, v_b_ada, v_norm_mix_w, v_norm_ffn_w, v_w_in, v_hgrn_lb_logits, v_hgrn_norm_w, v_q_norm_w, v_k_norm_w, v_attn_sinks, v_w_branch_hgrn, v_w_branch_attn, v_w_out, v_w_ffn_gate, v_w_ffn_up, v_w_ffn_down):
    xi, yi, ci = _place()
    chip = 2 * xi + yi
    dev = 2 * chip + ci
    s_len = x.shape[1]

    shards = [w_in[0].T, w_branch_hgrn[0], w_branch_attn[0], w_out[0], w_ffn_gate[0].T, w_ffn_up[0].T,
              w_ffn_down[0]]
    bufs = _cast_place(shards, ci, dev)

    lbrow = jnp.pad(hgrn_lb_logits.reshape(1, 512), ((0, 0), (0, D - 512)))
    blk = jnp.concatenate([c, lbrow, jnp.zeros((6, D), F32)], axis=0)
    nada = w_ada.shape[2]
    b_sh = lax.dynamic_slice(b_ada, (0, chip * nada), (1, nada))
    g0, c16, g1, in8 = _prologue(blk, c_ctx[None], w_ada[0], b_sh, bufs[0])
    lg = g0[0::2, 1, :512].reshape(4, 2, 2, 128).transpose(1, 2, 0, 3).reshape(2, 2, HGW)
    modall = g1[0::2].transpose(1, 0, 2).reshape(16, 4 * nada)
    mod = lax.dynamic_slice(modall, (dev, 0), (1, 6 * D)).reshape(6, D)
    modc = modall[8].reshape(6, D)[:2]

    sq, gx, _, small, rs = _local_step(
        x[0], ctx[0], loss_target[0], mod, modc, norm_mix_w, norm_ffn_w, lg, hgrn_norm_w, q_norm_w,
        k_norm_w, attn_sinks[0], in8.reshape(NCOL, D), bufs[1:], dist=(ci, chip))

    def whole(r):
        return r.reshape(2 * r.shape[1], r.shape[2])

    g_dn, g_g, g_u = [whole(r) for r in rs["ffn_done"]]
    g_bh, g_ba, g_o = [whole(r) for r in rs["mix_done"]]
    in_pairs = rs["in_pairs"]

    (g2, tot), in_part = _ag_small(small["raw"], sq, in_pairs, rs["small_rows"], rs["in_part"][0])
    loss = 0.5 * jnp.sum(tot[24]) / D
    dmodc_tot = jnp.pad(tot[6:8].reshape(1, 2 * D), ((0, 0), (0, 4 * D)))
    g_b_ada = tot[0:6].reshape(1, 6 * D) + dmodc_tot
    dmod16 = jnp.concatenate([g2[:, 0:6].reshape(8, 6 * D), dmodc_tot, jnp.zeros((7, 6 * D), F32)], axis=0)
    (g_w_ada, gc_part), in_contribs = _ada_bwd(
        c16, lax.dynamic_slice(dmod16, (0, chip * nada), (16, nada)), w_ada[0],
        carry=_carry_chipx(in_pairs, rows=rs["rest_rows"], into=[in_part]))
    in_reds, (g3,) = _rs_sibling_gather(_rs_chip_add(in_pairs, in_contribs, ci, chip), [gc_part[8:16]])
    g_in = whole(in_reds[0])
    g_c_ctx = _cctx_grad(g3[0::2, 0], c_ctx[None])[0]
    g_nw1 = tot[8:9]
    g_nw2 = tot[9:10]
    g_hw = tot[10, :HGW].reshape(4, HGD).sum(0, keepdims=True)
    g_qnw = tot[10, HGW:].reshape(8, HDIM).sum(0, keepdims=True)
    g_knw = tot[11, :128].reshape(2, HDIM).sum(0, keepdims=True)
    g_sinks = tot[16:24, 0][None]
    g_lg = lax.dynamic_slice(tot[12:16, :HGW].reshape(2, 2, HGW), (0, 0, chip * 128), (2, 2, 128))

    names = ["c_ctx", "w_ada", "b_ada", "norm_mix_w", "norm_ffn_w", "w_in", "hgrn_lb_logits", "hgrn_norm_w",
             "q_norm_w", "k_norm_w", "attn_sinks", "w_branch_hgrn", "w_branch_attn", "w_out", "w_ffn_gate",
             "w_ffn_up", "w_ffn_down"]
    ws = dict(zip(names, [c_ctx, w_ada, b_ada, norm_mix_w, norm_ffn_w, w_in, hgrn_lb_logits, hgrn_norm_w,
                          q_norm_w, k_norm_w, attn_sinks, w_branch_hgrn, w_branch_attn, w_out, w_ffn_gate,
                          w_ffn_up, w_ffn_down]))
    ms = dict(zip(names, [m_c_ctx, m_w_ada, m_b_ada, m_norm_mix_w, m_norm_ffn_w, m_w_in, m_hgrn_lb_logits,
                          m_hgrn_norm_w, m_q_norm_w, m_k_norm_w, m_attn_sinks, m_w_branch_hgrn,
                          m_w_branch_attn, m_w_out, m_w_ffn_gate, m_w_ffn_up, m_w_ffn_down]))
    vs = dict(zip(names, [v_c_ctx, v_w_ada, v_b_ada, v_norm_mix_w, v_norm_ffn_w, v_w_in, v_hgrn_lb_logits,
                          v_hgrn_norm_w, v_q_norm_w, v_k_norm_w, v_attn_sinks, v_w_branch_hgrn,
                          v_w_branch_attn, v_w_out, v_w_ffn_gate, v_w_ffn_up, v_w_ffn_down]))
    transposed = ("w_in", "w_ffn_gate", "w_ffn_up")

    def view(a, n):
        return a[0].T if n in transposed else a[0]

    def unview(a, n):
        return a.T[None] if n in transposed else a[None]

    delta, new_m, new_v, grads = {}, {}, {}, {}

    def big_adamw(group, gs, name):
        g_, d_, m_, v_ = _adamw_multi([view(ws[n], n) for n in group], gs, [view(ms[n], n) for n in group],
                                      [view(vs[n], n) for n in group], name=name)
        for i, n in enumerate(group):
            grads[n], delta[n], new_m[n], new_v[n] = (unview(g_[i], n), unview(d_[i], n), unview(m_[i], n),
                                                      unview(v_[i], n))

    big_adamw(["w_ffn_down", "w_ffn_gate", "w_ffn_up", "w_out", "w_branch_hgrn", "w_branch_attn"],
              [g_dn, g_g, g_u, g_o, g_bh, g_ba], "adamw_first")
    big_adamw(["w_in", "w_ada"], [g_in, g_w_ada], "adamw_second")
    grads.update(c_ctx=g_c_ctx, b_ada=g_b_ada, norm_mix_w=g_nw1, norm_ffn_w=g_nw2, hgrn_lb_logits=g_lg,
                 hgrn_norm_w=g_hw, q_norm_w=g_qnw, k_norm_w=g_knw, attn_sinks=g_sinks)
    small_names = [n for n in names if n not in delta]

    def two_d(a):
        return a.reshape(1, -1) if a.ndim == 1 else a

    sd, sm_, sv = _adamw_small(*[[two_d(d[n]) for n in small_names] for d in (ws, grads, ms, vs)])
    for i, n in enumerate(small_names):
        for dst, src in ((delta, sd), (new_m, sm_), (new_v, sv)):
            dst[n] = src[i].reshape(ws[n].shape)
    return (loss, gx[None], *[grads[n] for n in names], *[delta[n] for n in names],
            *[new_m[n] for n in names], *[new_v[n] for n in names])
```

```python
import functools

import numpy as np
import jax
import jax.numpy as jnp
from jax import lax
from jax.experimental import pallas as pl
from jax.experimental.pallas import tpu as pltpu

F32 = jnp.float32
BF16 = jnp.bfloat16
HI = lax.Precision.HIGHEST
MESH = pl.DeviceIdType.MESH

D = 1024
L = 256
TM = 256
HGW = 512
HGD = 128
CH = 32
ATW = 512
HDIM = 64
BLK = 128
GRID_W = 64
DFF = 2816
NCOL = 5376
EPS = 1e-6
ROPE_THETA = 10000.0
BF16_SUBLANES = 16

C_FB, C_INP, C_QHG, C_FF = 0, 1, 2, 3
C_GATES = 1
C_GHG, C_QRAW = 8, 9
C_KV = 20
C_QKV = 6

ADAM_LR, ADAM_B1, ADAM_B2, ADAM_EPS, ADAM_WD, ADAM_STEP = 0.001, 0.9, 0.999, 1e-08, 0.01, 10

NN = (((1,), (0,)), ((), ()))
NT = (((1,), (1,)), ((), ()))
TN = (((0,), (0,)), ((), ()))


def _dot(a, b, dims=NN, prec=None):
    return lax.dot_general(a, b, dims, precision=prec, preferred_element_type=F32)


def _bdot(a, b, dims=NN):
    return _dot(a.astype(BF16), b.astype(BF16), dims)


def _sig(x):
    return 1.0 / (1.0 + jnp.exp(-x))


class _Carry:
    def __init__(self, ins, outs, aliases, scratch, phases, peers):
        self.ins, self.outs, self.aliases, self.scratch, self.phases = ins, outs, aliases, scratch, phases
        self.peers = peers


BARRIER_IDS = {"sib": 1, "chips": 2, "both": 3}


def _peer_barrier(kind):
    x, y, c = _place()
    peers = []
    if kind in ("sib", "both"):
        peers.append((x, y, 1 - c))
    if kind in ("chips", "both"):
        peers += [(1 - x, y, c), (x, 1 - y, c), (1 - x, 1 - y, c)]
    bar = pltpu.get_barrier_semaphore()
    for peer in peers:
        pl.semaphore_signal(bar, inc=1, device_id=peer, device_id_type=MESH)
    pl.semaphore_wait(bar, len(peers))


def _in_hbm(args):
    return [pltpu.with_memory_space_constraint(a, pltpu.HBM) for a in args]


def _out_hbm(shapes):
    if isinstance(shapes, (list, tuple)):
        return [pltpu.HBM(s.shape, s.dtype) for s in shapes]
    return pltpu.HBM(shapes.shape, shapes.dtype)


def _carry_join(a, b):
    na_in, na_out, na_sc = len(a.ins), len(a.outs), len(a.scratch)
    aliases = dict(a.aliases)
    aliases.update({na_in + i: na_out + o for i, o in b.aliases.items()})

    def phases(ins, outs, sems):
        pa = a.phases(ins[:na_in], outs[:na_out], sems[:na_sc])
        pb = b.phases(ins[na_in:], outs[na_out:], sems[na_sc:])

        def both(fa, fb):
            if fa is None and fb is None:
                return None

            def run():
                for fn in (fa, fb):
                    if fn is not None:
                        fn()
            return run

        return tuple(both(fa, fb) for fa, fb in zip(pa, pb))

    return _Carry(list(a.ins) + list(b.ins), list(a.outs) + list(b.outs), aliases,
                  list(a.scratch) + list(b.scratch), phases, a.peers if a.peers == b.peers else "both")


def _pcall(body, *, name, grid, in_specs, out_specs, out_shape, scratch=(), aliases=None, vmem_mb=48,
           carry=None):
    params = pltpu.CompilerParams(dimension_semantics=("arbitrary",) * len(grid),
                                  vmem_limit_bytes=vmem_mb << 20)
    if carry is None:
        plain = pl.pallas_call(
            body, name=name, grid=grid, in_specs=in_specs, out_specs=out_specs, out_shape=_out_hbm(out_shape),
            scratch_shapes=list(scratch), input_output_aliases=aliases or {}, compiler_params=params)
        return lambda *args: plain(*_in_hbm(args))
    single = not isinstance(out_shape, (list, tuple))
    out_specs_l = [out_specs] if single else list(out_specs)
    out_shape_l = [out_shape] if single else list(out_shape)
    n_in, n_out, n_sc = len(in_specs), len(out_shape_l), len(scratch)
    k_in, k_out = len(carry.ins), len(carry.outs)
    nsteps = int(np.prod(grid))
    assert nsteps >= 3

    def wrapped(*refs):
        ins, cins = refs[:n_in], refs[n_in:n_in + k_in]
        o0 = n_in + k_in
        outs, couts = refs[o0:o0 + n_out], refs[o0 + n_out:o0 + n_out + k_out]
        s0 = o0 + n_out + k_out
        sc, csc = refs[s0:s0 + n_sc], refs[s0 + n_sc:]
        step = pl.program_id(0)
        for ax in range(1, len(grid)):
            step = step * grid[ax] + pl.program_id(ax)
        start, mid, end = carry.phases(cins, couts, csc)

        @pl.when(step == 0)
        def _():
            _peer_barrier(carry.peers)
            start()

        body(*ins, *outs, *sc)
        if mid is not None:
            pl.when(step == nsteps - 2)(mid)
        pl.when(step == nsteps - 1)(end)

    all_aliases = dict(aliases or {})
    all_aliases.update({n_in + i: n_out + o for i, o in carry.aliases.items()})
    call = pl.pallas_call(
        wrapped, name=name, grid=grid, in_specs=list(in_specs) + [ANY] * k_in,
        out_specs=out_specs_l + [ANY] * k_out, out_shape=_out_hbm(out_shape_l + list(carry.outs)),
        scratch_shapes=list(scratch) + list(carry.scratch), input_output_aliases=all_aliases,
        compiler_params=pltpu.CompilerParams(dimension_semantics=("arbitrary",) * len(grid),
                                             vmem_limit_bytes=vmem_mb << 20,
                                             collective_id=BARRIER_IDS[carry.peers]))

    def run(*args):
        res = call(*_in_hbm(args), *_in_hbm(carry.ins))
        core = res[:n_out]
        return (core[0] if single else list(core)), list(res[n_out:])

    return run


def _full(shape):
    nd = len(shape)
    return pl.BlockSpec(shape, lambda *_: (0,) * nd)


ANY = pl.BlockSpec(memory_space=pl.ANY)


NT_IN = NCOL // 256


def _src_block(j):
    return j + jnp.where(j < 4, 2, jnp.where(j < 6, 3, jnp.where(j < 8, -6, jnp.where(
        j < 16, 5, jnp.where(j < 20, -7, -14)))))


def _mm_in(h, wt, tm, carry=None):
    tt = h.shape[0]

    def body(h_ref, w_ref, o_ref):
        o_ref[...] = _bdot(h_ref[...], w_ref[...], NT)

    return _pcall(body, name="mm_in", grid=(tt // tm, NT_IN),
                  in_specs=[pl.BlockSpec((tm, D), lambda i, j: (i, 0)),
                            pl.BlockSpec((256, D), lambda i, j: (_src_block(j), 0))],
                  out_specs=pl.BlockSpec((tm, 256), lambda i, j: (i, j)),
                  out_shape=jax.ShapeDtypeStruct((tt, NCOL), F32), carry=carry)(h, wt)


def _mm_dh(dp, wt, tm, carry=None):
    tt = dp.shape[0]
    per, ng = 3, NT_IN // 3

    def body(d_ref, w0, w1, w2, o_ref, acc):
        kk = pl.program_id(1)

        @pl.when(kk == 0)
        def _():
            acc[...] = jnp.zeros_like(acc)

        acc[...] += (_bdot(d_ref[:, 0:256], w0[...]) + _bdot(d_ref[:, 256:512], w1[...])
                     + _bdot(d_ref[:, 512:768], w2[...]))

        @pl.when(kk == ng - 1)
        def _():
            o_ref[...] = acc[...]

    wspecs = [pl.BlockSpec((256, D), functools.partial(lambda t, i, kk: (_src_block(per * kk + t), 0), t))
              for t in range(per)]
    return _pcall(body, name="mm_dh", grid=(tt // tm, ng),
                  in_specs=[pl.BlockSpec((tm, per * 256), lambda i, kk: (i, kk))] + wspecs,
                  out_specs=pl.BlockSpec((tm, D), lambda i, kk: (i, 0)),
                  out_shape=jax.ShapeDtypeStruct((tt, D), F32), scratch=[pltpu.VMEM((tm, D), F32)],
                  carry=carry)(dp, wt, wt, wt)


def _mm_gin(dp, h, tk):
    tt = dp.shape[0]
    nk = tt // tk

    def body(d_ref, h_ref, o_ref, acc):
        kk = pl.program_id(1)

        @pl.when(kk == 0)
        def _():
            acc[...] = jnp.zeros_like(acc)

        acc[...] += _bdot(d_ref[...], h_ref[...], TN)

        @pl.when(kk == nk - 1)
        def _():
            o_ref[...] = acc[...].astype(BF16)

    return _pcall(body, name="mm_gin", grid=(NT_IN, nk),
                  in_specs=[pl.BlockSpec((tk, 256), lambda j, kk: (kk, j)),
                            pl.BlockSpec((tk, D), lambda j, kk: (kk, 0))],
                  out_specs=pl.BlockSpec((256, D), lambda j, kk: (_src_block(j), 0)),
                  out_shape=jax.ShapeDtypeStruct((NCOL, D), BF16), scratch=[pltpu.VMEM((256, D), F32)])(dp, h)


def _tok_specs():
    assert L == TM
    return [_full((TM, D)), pl.BlockSpec((TM, D), lambda i: (jnp.maximum(i - 1, 0), 0))]


def _mod1(ctx, x, nw, ss):
    rows = L + x.shape[0]

    def body(c_ref, x_ref, nw_ref, ss_ref, h_ref):
        t = jnp.where(pl.program_id(0) == 0, c_ref[...], x_ref[...])
        r = lax.rsqrt(jnp.mean(t * t, axis=-1, keepdims=True) + EPS)
        s = ss_ref[0]
        h_ref[...] = ((t * r * nw_ref[...]) * (1.0 + s[1:2]) + s[0:1]).astype(BF16)

    return _pcall(body, name="mod1", grid=(rows // TM,),
                  in_specs=_tok_specs() + [_full((1, D)),
                                           pl.BlockSpec((1, 2, D), lambda i: (jnp.minimum(i, 1), 0, 0))],
                  out_specs=pl.BlockSpec((TM, D), lambda i: (i, 0)),
                  out_shape=jax.ShapeDtypeStruct((rows, D), BF16))(ctx, x, nw, ss)


def _norm_bwd_rows(x, dh, nw, scale):
    r = lax.rsqrt(jnp.mean(x * x, axis=-1, keepdims=True) + EPS)
    xh = x * r
    dxh = dh * ((1.0 + scale) * nw)
    dx = r * (dxh - xh * jnp.mean(dxh * xh, axis=-1, keepdims=True))
    return dx, xh


def _out_proj_mod2(mixed, w_o, x, g1, nw2, ss2):
    s_len = x.shape[0]
    tm = 512

    def body(m_ref, w_ref, x_ref, g_ref, nw_ref, ss_ref, ao_ref, x1_ref, h_ref):
        ao = _bdot(m_ref[...], w_ref[...])
        ao_ref[...] = ao.astype(BF16)
        x1 = x_ref[...] + g_ref[...] * ao
        x1_ref[...] = x1
        r = lax.rsqrt(jnp.mean(x1 * x1, axis=-1, keepdims=True) + EPS)
        s = ss_ref[0]
        h_ref[...] = ((x1 * r * nw_ref[...]) * (1.0 + s[1:2]) + s[0:1]).astype(BF16)

    row = pl.BlockSpec((tm, D), lambda i: (i, 0))
    f = jax.ShapeDtypeStruct((s_len, D), F32)
    return _pcall(body, name="out_proj_mod2", grid=(s_len // tm,),
                  in_specs=[row, _full((D, D)), row, _full((1, D)), _full((1, D)), _full((1, 2, D))],
                  out_specs=[row, row, row],
                  out_shape=[jax.ShapeDtypeStruct((s_len, D), BF16), f,
                             jax.ShapeDtypeStruct((s_len, D), BF16)])(mixed, w_o, x, g1, nw2, ss2)


TS = 1024


def _acc_call(body, *, name, grid, in_specs, out_specs, out_shape, acc_shapes, args, carry=None):
    return _pcall(body, name=name, grid=grid, in_specs=in_specs, out_specs=out_specs, out_shape=out_shape,
                  scratch=[pltpu.VMEM(s, F32) for s in acc_shapes], carry=carry)(*args)


def _ffn_up(h2, g4, u4, carry=None):
    s_len = h2.shape[0]
    ns = g4.shape[1]

    def body(h_ref, g_ref, u_ref, a_ref, b_ref, z_ref):
        h = h_ref[...]
        j = pl.program_id(1)
        a = _bdot(h, g_ref[j], NT)
        b = _bdot(h, u_ref[j], NT)
        a_ref[0] = a.astype(BF16)
        b_ref[0] = b.astype(BF16)
        z_ref[0] = (a * _sig(a) * b).astype(BF16)

    w = _full((4, ns, D))
    o = pl.BlockSpec((1, TS, ns), lambda i, j: (j, i, 0))
    f = jax.ShapeDtypeStruct((4, s_len, ns), BF16)
    return _pcall(body, name="ffn_up", grid=(s_len // TS, 4),
                  in_specs=[pl.BlockSpec((TS, D), lambda i, j: (i, 0)), w, w], out_specs=[o, o, o],
                  out_shape=[f, f, jax.ShapeDtypeStruct((4, s_len, ns), BF16)], vmem_mb=56,
                  carry=carry)(h2, g4, u4)


def _ffn_down_loss(z4, dn4, x1, g2, tgt):
    _, s_len, ns = z4.shape

    def body(z_ref, w_ref, x1_ref, g_ref, t_ref, sq_ref, dx2_ref, dyb_ref, dg_ref, acc):
        i, j = pl.program_id(0), pl.program_id(1)

        @pl.when((i == 0) & (j == 0))
        def _():
            sq_ref[...] = jnp.zeros_like(sq_ref)
            dg_ref[...] = jnp.zeros_like(dg_ref)

        @pl.when(j == 0)
        def _():
            acc[...] = jnp.zeros_like(acc)

        acc[...] += _bdot(z_ref[0], w_ref[0])

        @pl.when(j == 3)
        def _():
            y_ = acc[...]
            g = g_ref[...]
            e = x1_ref[...] + g * y_ - t_ref[...]
            sq_ref[...] += jnp.sum(e * e, axis=0, keepdims=True)
            dx2 = e * (1.0 / D)
            dx2_ref[...] = dx2
            dyb_ref[...] = (g * dx2).astype(BF16)
            dg_ref[...] += jnp.sum(dx2 * y_, axis=0, keepdims=True)

    row = pl.BlockSpec((TS, D), lambda i, j: (i, 0))
    vec = _full((1, D))
    return _acc_call(body, name="ffn_down_loss", grid=(s_len // TS, 4),
                     in_specs=[pl.BlockSpec((1, TS, ns), lambda i, j: (j, i, 0)),
                               pl.BlockSpec((1, ns, D), lambda i, j: (j, 0, 0)), row, vec, row],
                     out_specs=[vec, row, row, vec],
                     out_shape=[jax.ShapeDtypeStruct((1, D), F32), jax.ShapeDtypeStruct((s_len, D), F32),
                                jax.ShapeDtypeStruct((s_len, D), BF16), jax.ShapeDtypeStruct((1, D), F32)],
                     acc_shapes=[(TS, D)], args=(z4, dn4, x1, g2, tgt))


def _ffn_dz(dyb, dn4, a4, b4):
    _, s_len, ns = a4.shape

    def body(dy_ref, w_ref, a_ref, b_ref, da_ref, db_ref):
        dz = _bdot(dy_ref[...], w_ref[0], NT)
        a = a_ref[0].astype(F32)
        s = _sig(a)
        da_ref[0] = (dz * b_ref[0].astype(F32) * (s * (1.0 + a * (1.0 - s)))).astype(BF16)
        db_ref[0] = (dz * (a * s)).astype(BF16)

    t = pl.BlockSpec((1, TS, ns), lambda i, j: (j, i, 0))
    o = jax.ShapeDtypeStruct((4, s_len, ns), BF16)
    return _pcall(body, name="ffn_dz", grid=(s_len // TS, 4),
                  in_specs=[pl.BlockSpec((TS, D), lambda i, j: (i, 0)),
                            pl.BlockSpec((1, ns, D), lambda i, j: (j, 0, 0)), t, t],
                  out_specs=[t, t], out_shape=[o, o])(dyb, dn4, a4, b4)


def _ffn_gdn(z4, dyb):
    _, s_len, ns = z4.shape
    tk = min(s_len, 2 * TS)
    nk = s_len // tk

    def body(z_ref, dy_ref, o_ref, acc):
        t = pl.program_id(1)

        @pl.when(t == 0)
        def _():
            acc[...] = jnp.zeros_like(acc)

        acc[...] += _bdot(z_ref[0], dy_ref[...], TN)

        @pl.when(t == nk - 1)
        def _():
            o_ref[0] = acc[...].astype(o_ref.dtype)

    return _acc_call(body, name="ffn_gdn", grid=(4, nk),
                     in_specs=[pl.BlockSpec((1, tk, ns), lambda j, t: (j, t, 0)),
                               pl.BlockSpec((tk, D), lambda j, t: (t, 0))],
                     out_specs=pl.BlockSpec((1, ns, D), lambda j, t: (j, 0, 0)),
                     out_shape=jax.ShapeDtypeStruct((4, ns, D), BF16), acc_shapes=[(ns, D)], args=(z4, dyb))


def _ffn_dh2(da4, db4, g4, u4, carry=None):
    _, s_len, ns = da4.shape

    def body(da_ref, db_ref, g_ref, u_ref, o_ref, acc):
        j = pl.program_id(1)

        @pl.when(j == 0)
        def _():
            acc[...] = jnp.zeros_like(acc)

        acc[...] += _bdot(da_ref[0], g_ref[0]) + _bdot(db_ref[0], u_ref[0])

        @pl.when(j == 3)
        def _():
            o_ref[...] = acc[...]

    t = pl.BlockSpec((1, TS, ns), lambda i, j: (j, i, 0))
    w = pl.BlockSpec((1, ns, D), lambda i, j: (j, 0, 0))
    return _acc_call(body, name="ffn_dh2", grid=(s_len // TS, 4), in_specs=[t, t, w, w],
                     out_specs=pl.BlockSpec((TS, D), lambda i, j: (i, 0)),
                     out_shape=jax.ShapeDtypeStruct((s_len, D), F32), acc_shapes=[(TS, D)],
                     args=(da4, db4, g4, u4), carry=carry)


def _ffn_ggu(h2, da4, db4, carry=None):
    _, s_len, ns = da4.shape
    nk = s_len // TS

    def body(h_ref, da_ref, db_ref, gg_ref, gu_ref, acc_g, acc_u):
        t = pl.program_id(1)

        @pl.when(t == 0)
        def _():
            acc_g[...] = jnp.zeros_like(acc_g)
            acc_u[...] = jnp.zeros_like(acc_u)

        h = h_ref[...]
        acc_g[...] += _bdot(da_ref[0], h, TN)
        acc_u[...] += _bdot(db_ref[0], h, TN)

        @pl.when(t == nk - 1)
        def _():
            gg_ref[0] = acc_g[...].astype(BF16)
            gu_ref[0] = acc_u[...].astype(BF16)

    d = pl.BlockSpec((1, TS, ns), lambda j, t: (j, t, 0))
    o = pl.BlockSpec((1, ns, D), lambda j, t: (j, 0, 0))
    f = jax.ShapeDtypeStruct((4, ns, D), BF16)
    return _acc_call(body, name="ffn_ggu", grid=(4, nk),
                     in_specs=[pl.BlockSpec((TS, D), lambda j, t: (t, 0)), d, d], out_specs=[o, o],
                     out_shape=[f, f], acc_shapes=[(ns, D), (ns, D)], args=(h2, da4, db4), carry=carry)


def _mod2_bwd(x1, dh2, dx2, ao, nw2, ss2, g1):
    s_len = x1.shape[0]

    def body(x1_ref, dh_ref, dx2_ref, ao_ref, nw_ref, ss_ref, g_ref,
             dx1_ref, da_ref, dss_ref, dnw_ref, dg_ref):
        i = pl.program_id(0)

        @pl.when(i == 0)
        def _():
            dss_ref[...] = jnp.zeros_like(dss_ref)
            dnw_ref[...] = jnp.zeros_like(dnw_ref)
            dg_ref[...] = jnp.zeros_like(dg_ref)

        dh = dh_ref[...]
        nw = nw_ref[...]
        scale = ss_ref[0][1:2]
        dxn, xh = _norm_bwd_rows(x1_ref[...], dh, nw, scale)
        dx1 = dx2_ref[...] + dxn
        dx1_ref[...] = dx1
        da_ref[...] = (g_ref[...] * dx1).astype(BF16)
        dg_ref[...] += jnp.sum(dx1 * ao_ref[...].astype(F32), axis=0, keepdims=True)
        dsh = jnp.sum(dh, axis=0, keepdims=True)
        dsc = jnp.sum(dh * xh * nw, axis=0, keepdims=True)
        dss_ref[...] += jnp.concatenate([dsh, dsc], axis=0)
        dnw_ref[...] += jnp.sum(dh * xh * (1.0 + scale), axis=0, keepdims=True)

    row = pl.BlockSpec((TM, D), lambda i: (i, 0))
    vec = _full((1, D))
    return _pcall(body, name="mod2_bwd", grid=(s_len // TM,),
                  in_specs=[row, row, row, row, vec, _full((1, 2, D)), vec],
                  out_specs=[row, row, _full((2, D)), vec, vec],
                  out_shape=[jax.ShapeDtypeStruct((s_len, D), F32), jax.ShapeDtypeStruct((s_len, D), BF16),
                             jax.ShapeDtypeStruct((2, D), F32), jax.ShapeDtypeStruct((1, D), F32),
                             jax.ShapeDtypeStruct((1, D), F32)])(x1, dh2, dx2, ao, nw2, ss2, g1)


def _mod1_bwd(ctx, x, dh, dx1, nw1, ss1):
    s_len = dx1.shape[0]
    tt = L + s_len

    def body(c_ref, x_ref, dh_ref, dx1_ref, nw_ref, ss_ref, dx_ref, dss_ref, dnw_ref):
        i = pl.program_id(0)
        tok = jnp.where(i == 0, c_ref[...], x_ref[...])

        @pl.when(i == 0)
        def _():
            dnw_ref[...] = jnp.zeros_like(dnw_ref)

        @pl.when(i <= 1)
        def _():
            dss_ref[...] = jnp.zeros_like(dss_ref)

        dh_ = dh_ref[...]
        nw = nw_ref[...]
        scale = ss_ref[0][1:2]
        dxn, xh = _norm_bwd_rows(tok, dh_, nw, scale)

        @pl.when(i >= 1)
        def _():
            dx_ref[...] = dx1_ref[...] + dxn

        dsh = jnp.sum(dh_, axis=0, keepdims=True)
        dsc = jnp.sum(dh_ * xh * nw, axis=0, keepdims=True)
        dss_ref[...] += jnp.concatenate([dsh, dsc], axis=0)[None]
        dnw_ref[...] += jnp.sum(dh_ * xh * (1.0 + scale), axis=0, keepdims=True)

    row = pl.BlockSpec((TM, D), lambda i: (i, 0))
    lat = pl.BlockSpec((TM, D), lambda i: (jnp.maximum(i - 1, 0), 0))
    sel = pl.BlockSpec((1, 2, D), lambda i: (jnp.minimum(i, 1), 0, 0))
    return _pcall(body, name="mod1_bwd", grid=(tt // TM,),
                  in_specs=_tok_specs() + [row, lat, _full((1, D)), sel],
                  out_specs=[lat, sel, _full((1, D))],
                  out_shape=[jax.ShapeDtypeStruct((s_len, D), F32), jax.ShapeDtypeStruct((2, 2, D), F32),
                             jax.ShapeDtypeStruct((1, D), F32)])(ctx, x, dh, dx1, nw1, ss1)


def _rows(c):
    return slice(c * CH, (c + 1) * CH)


def _chunk_masks(rev, transpose=False):
    r = lax.broadcasted_iota(jnp.int32, (TM, TM), 0)
    c = lax.broadcasted_iota(jnp.int32, (TM, TM), 1)
    same = (r // CH) == (c // CH)
    before = (c >= r) if (rev != transpose) else (c <= r)
    return same & before, same


def _chunk_scan(x, rev, transpose=False):
    r = lax.broadcasted_iota(jnp.int32, (CH, CH), 0)
    c = lax.broadcasted_iota(jnp.int32, (CH, CH), 1)
    tri = ((c >= r) if (rev != transpose) else (c <= r)).astype(F32)
    return jnp.concatenate([_dot(tri, x[_rows(ch)], prec=HI) for ch in range(x.shape[0] // CH)], axis=0)


def _chunk_total(x):
    return jnp.concatenate([jnp.broadcast_to(jnp.sum(x[_rows(ch)], axis=0, keepdims=True), (CH, x.shape[1]))
                            for ch in range(x.shape[0] // CH)], axis=0)


def _hgrn_gate(fl, qraw, lg):
    lb = 1.0 / (1.0 + jnp.exp(lg[1:2] - lg[0:1]))
    sg = _sig(fl)
    f = lb + (1.0 - lb) * sg
    q = qraw * _sig(qraw) * (HGD ** -0.5)
    return lb, sg, f, q


def _hgrn_fwd(p, lg, *, rev, carry=None, readout=None):
    tt = p.shape[0]
    nt = tt // TM
    ncht = TM // CH
    d = 1 if rev else 0

    def tile_of(s):
        return jnp.where(s == 0, 0, nt - s) if rev else s

    def body(*refs):
        if readout is None:
            f_ref, inp_ref, q_ref, lg_ref, o_ref, st_ref, state = refs
        else:
            f_ref, inp_ref, q_ref, lg_ref, oo_ref, g_ref, hw_ref, o_ref, st_ref, y_ref, state = refs
        s = pl.program_id(0)

        @pl.when(s == 0)
        def _():
            state[...] = jnp.zeros_like(state)

        _, _, f, q = _hgrn_gate(f_ref[...], q_ref[...], lg_ref[0])
        lf = jnp.log(f)
        causal, _ = _chunk_masks(rev)
        cum = _chunk_scan(lf, rev)
        tot = _chunk_total(lf)
        qd = (q * jnp.exp(cum)).astype(BF16)
        kd = ((1.0 - f) * jnp.exp(-cum)).astype(BF16)
        ke = ((1.0 - f) * jnp.exp(tot - cum)).astype(BF16)
        et = jnp.exp(tot)
        v = inp_ref[...].astype(BF16)
        order = range(ncht - 1, -1, -1) if rev else range(ncht)
        outs = []
        for h in range(4):
            sl = slice(h * HGD, (h + 1) * HGD)
            qd_, kd_, ke_, v_ = qd[:, sl], kd[:, sl], ke[:, sl], v[:, sl]
            pm = jnp.where(causal, _dot(qd_, kd_, NT), 0.0).astype(BF16)
            o_h = _dot(pm, v_)
            upd = [_dot(v_[_rows(c)], ke_[_rows(c)], TN) for c in range(ncht)]
            st = state[h]
            for c in order:
                st_ref[c, h] = st
                st = st * et[c * CH:c * CH + 1, sl] + upd[c]
            state[h] = st
            inter = [_dot(qd_[_rows(c)], st_ref[c, h].astype(BF16), NT) for c in range(ncht)]
            outs.append(o_h + jnp.concatenate(inter, axis=0))
        o_tile = jnp.concatenate(outs, axis=1)
        o_ref[...] = o_tile
        if readout is not None:
            @pl.when(tile_of(s) >= 1)
            def _():
                g = g_ref[...]
                y_ref[...] = (_head_rms(oo_ref[...] + o_tile, None, 4) * hw_ref[...] * (g * _sig(g))).astype(BF16)

    def col(cb):
        return pl.BlockSpec((TM, HGW), lambda s: (tile_of(s), cb))

    in_specs = [col(C_FB if rev else C_FF), col(C_INP), col(C_QHG), pl.BlockSpec((1, 2, HGW), lambda s: (d, 0, 0))]
    out_specs = [col(0), pl.BlockSpec((ncht, 4, HGD, HGD), lambda s: (tile_of(s), 0, 0, 0))]
    out_shape = [jax.ShapeDtypeStruct((tt, HGW), F32), jax.ShapeDtypeStruct((nt * ncht, 4, HGD, HGD), F32)]
    args = [p, p, p, lg]
    if readout is not None:
        in_specs += [col(0), col(C_GHG), _full((1, HGW))]
        args += [readout[0], p, readout[1]]
        assert rev
        out_specs.append(pl.BlockSpec((TM, HGW), lambda s: (jnp.where(s == 0, nt - 2, tile_of(s) - 1), 0)))
        out_shape.append(jax.ShapeDtypeStruct((tt - L, HGW), BF16))
    return _pcall(body, name="hgrn_fwd_rev" if rev else "hgrn_fwd", grid=(nt,), in_specs=in_specs,
                  out_specs=out_specs, out_shape=out_shape, scratch=[pltpu.VMEM((4, HGD, HGD), F32)],
                  carry=carry)(*args)


def _hgrn_bwd(p, lg, do, st, dp, prev, *, rev, carry=None):
    tt = p.shape[0]
    nt = tt // TM
    ncht = TM // CH
    d = 1 if rev else 0
    second = prev is not None

    def tile_of(s):
        return jnp.where(s == nt - 1, 0, s + 1) if rev else nt - 1 - s

    def body(*refs):
        if second:
            (f_ref, inp_ref, q_ref, lg_ref, do_ref, st_ref, dvp_ref, dqp_ref, _dp_in,
             dp_ref, dlg_ref, dstate) = refs
        else:
            (f_ref, inp_ref, q_ref, lg_ref, do_ref, st_ref, _dp_in,
             dp_ref, dv_ref, dq_ref, dlg_ref, dstate) = refs
        s = pl.program_id(0)
        tile = tile_of(s)

        @pl.when(s == 0)
        def _():
            dstate[...] = jnp.zeros_like(dstate)
            dlg_ref[...] = jnp.zeros_like(dlg_ref)

        qraw = q_ref[...]
        lb, sg, f, q = _hgrn_gate(f_ref[...], qraw, lg_ref[0])
        lf = jnp.log(f)
        causal, _ = _chunk_masks(rev)
        causal_t, _ = _chunk_masks(rev, transpose=True)
        cum = _chunk_scan(lf, rev)
        tot = _chunk_total(lf)
        ea, eb, ee, et = jnp.exp(cum), jnp.exp(-cum), jnp.exp(tot - cum), jnp.exp(tot)
        qdf, kdf, kef = q * ea, (1.0 - f) * eb, (1.0 - f) * ee
        qd, kd, ke = qdf.astype(BF16), kdf.astype(BF16), kef.astype(BF16)
        v = inp_ref[...].astype(BF16)
        dob = jnp.where(tile == 0, 0.0, do_ref[...]).astype(BF16)
        order = range(ncht) if rev else range(ncht - 1, -1, -1)
        dq_l, dk_l, dv_l, dcum_l, dtot_l = [], [], [], [], []
        for h in range(4):
            sl = slice(h * HGD, (h + 1) * HGD)
            qd_, kd_, ke_, v_, do_ = qd[:, sl], kd[:, sl], ke[:, sl], v[:, sl], dob[:, sl]
            pmt = jnp.where(causal_t, _dot(kd_, qd_, NT), 0.0).astype(BF16)
            dpm = jnp.where(causal, _dot(do_, v_, NT), 0.0).astype(BF16)
            dpmt = jnp.where(causal_t, _dot(v_, do_, NT), 0.0).astype(BF16)
            dv = _dot(pmt, do_)
            dqd = _dot(dpm, kd_)
            dkd = _dot(dpmt, qd_)
            upd = [_dot(do_[_rows(c)], qd_[_rows(c)], TN) for c in range(ncht)]
            ds = dstate[h]
            ds1 = [None] * ncht
            for c in order:
                ds1[c] = ds
                ds = ds * et[c * CH:c * CH + 1, sl] + upd[c]
            dstate[h] = ds
            dke_c, dv_c, dqd_c, dtot_c = [], [], [], []
            for c in range(ncht):
                st0 = st_ref[c, h]
                dsb = ds1[c].astype(BF16)
                dke_ = _dot(v_[_rows(c)], dsb)
                dke_c.append(dke_)
                dv_c.append(_dot(ke_[_rows(c)], dsb, NT))
                dqd_c.append(_dot(do_[_rows(c)], st0.astype(BF16)))
                dt = (jnp.sum(ds1[c] * st0, axis=0, keepdims=True) * et[c * CH:c * CH + 1, sl]
                      + jnp.sum(dke_ * kef[_rows(c), sl], axis=0, keepdims=True))
                dtot_c.append(jnp.broadcast_to(dt, (CH, HGD)))
            dke = jnp.concatenate(dke_c, axis=0)
            dqd = dqd + jnp.concatenate(dqd_c, axis=0)
            dv_l.append(dv + jnp.concatenate(dv_c, axis=0))
            dtot_l.append(jnp.concatenate(dtot_c, axis=0))
            dq_l.append(dqd * ea[:, sl])
            dk_l.append(dkd * eb[:, sl] + dke * ee[:, sl])
            dcum_l.append(dqd * qdf[:, sl] - dkd * kdf[:, sl] - dke * kef[:, sl])
        dcum = jnp.concatenate(dcum_l, axis=1)
        dlf = _chunk_scan(dcum, rev, transpose=True) + jnp.concatenate(dtot_l, axis=1)
        dq_t = jnp.concatenate(dq_l, axis=1)
        dv_t = jnp.concatenate(dv_l, axis=1)

        df = dlf / f - jnp.concatenate(dk_l, axis=1)
        dfl = df * (1.0 - lb) * sg * (1.0 - sg)
        dlb = jnp.sum(df * (1.0 - sg), axis=0, keepdims=True)
        dl0 = dlb * lb * (1.0 - lb)
        dlg_ref[...] += jnp.concatenate([dl0, -dl0], axis=0)[None]
        if second:
            sq = _sig(qraw)
            dqr = (dqp_ref[...] + dq_t) * (HGD ** -0.5) * (sq * (1.0 + qraw * (1.0 - sq)))
            dp_ref[...] = jnp.concatenate([dfl, dvp_ref[...] + dv_t, dqr], axis=1).astype(BF16)
        else:
            dp_ref[...] = dfl.astype(BF16)
            dv_ref[...] = dv_t
            dq_ref[...] = dq_t

    def col(cb):
        return pl.BlockSpec((TM, HGW), lambda s: (tile_of(s), cb))

    tok = pl.BlockSpec((TM, HGW), lambda s: (tile_of(s), 0))
    in_specs = [col(C_FB if rev else C_FF), col(C_INP), col(C_QHG),
                pl.BlockSpec((1, 2, HGW), lambda s: (d, 0, 0)),
                pl.BlockSpec((TM, HGW), lambda s: (jnp.maximum(tile_of(s) - 1, 0), 0)),
                pl.BlockSpec((ncht, 4, HGD, HGD), lambda s: (tile_of(s), 0, 0, 0))]
    args = [p, p, p, lg, do, st]
    dlg_spec = _full((1, 2, HGW))
    dlg_shape = jax.ShapeDtypeStruct((1, 2, HGW), F32)
    if second:
        in_specs += [tok, tok]
        args += [prev[0], prev[1]]
        out_specs = [pl.BlockSpec((TM, 3 * HGW), lambda s: (tile_of(s), 0)), dlg_spec]
        out_shape = [jax.ShapeDtypeStruct(dp.shape, BF16), dlg_shape]
    else:
        out_specs = [pl.BlockSpec((TM, HGW), lambda s: (tile_of(s), C_FB if rev else C_FF)), tok, tok, dlg_spec]
        out_shape = [jax.ShapeDtypeStruct(dp.shape, BF16), jax.ShapeDtypeStruct((tt, HGW), F32),
                     jax.ShapeDtypeStruct((tt, HGW), F32), dlg_shape]
    in_specs.append(ANY)
    args.append(dp)
    return _pcall(body, name="hgrn_bwd_rev" if rev else "hgrn_bwd", grid=(nt,),
                  in_specs=in_specs, out_specs=out_specs, out_shape=out_shape,
                  scratch=[pltpu.VMEM((4, HGD, HGD), F32)],
                  aliases={len(args) - 1: 0}, carry=carry)(*args)


def _head_rms(o, w, nheads):
    outs = []
    for h in range(nheads):
        oh = o[:, h * HGD:(h + 1) * HGD]
        outs.append(oh * lax.rsqrt(jnp.mean(oh * oh, axis=-1, keepdims=True) + EPS))
    return jnp.concatenate(outs, axis=1)


def _readout_bwd(o0, o1, p, hw4, dy, dp, carry=None):
    tt = o0.shape[0]
    s_len = tt - L

    def body(o0_ref, o1_ref, g_ref, w_ref, dy_ref, _dp_in, dp_ref, do_ref, dw_ref):
        i = pl.program_id(0)

        @pl.when(i == 0)
        def _():
            dw_ref[...] = jnp.zeros_like(dw_ref)
            dp_ref[...] = jnp.zeros_like(dp_ref)

        @pl.when(i >= 1)
        def _():
            o = o0_ref[...] + o1_ref[...]
            g = g_ref[...]
            w = w_ref[...]
            sg = _sig(g)
            dy_ = dy_ref[...]
            dsw = dy_ * (g * sg)
            outs, xhs = [], []
            for h in range(4):
                sl = slice(h * HGD, (h + 1) * HGD)
                oh = o[:, sl]
                r = lax.rsqrt(jnp.mean(oh * oh, axis=-1, keepdims=True) + EPS)
                xh = oh * r
                dxh = dsw[:, sl] * w[:, sl]
                outs.append(r * (dxh - xh * jnp.mean(dxh * xh, axis=-1, keepdims=True)))
                xhs.append(xh)
            xh = jnp.concatenate(xhs, axis=1)
            do_ref[...] = jnp.concatenate(outs, axis=1)
            dp_ref[...] = (dy_ * xh * w * (sg * (1.0 + g * (1.0 - sg)))).astype(BF16)
            dw_ref[...] += jnp.sum(dsw * xh, axis=0, keepdims=True)

    tok = pl.BlockSpec((TM, HGW), lambda i: (i, 0))
    lat = pl.BlockSpec((TM, HGW), lambda i: (jnp.maximum(i - 1, 0), 0))
    return _pcall(body, name="readout_bwd", grid=(tt // TM,),
                  in_specs=[tok, tok, pl.BlockSpec((TM, HGW), lambda i: (i, C_GHG)), _full((1, HGW)), lat, ANY],
                  out_specs=[pl.BlockSpec((TM, HGW), lambda i: (i, C_GHG)), lat, _full((1, HGW))],
                  out_shape=[jax.ShapeDtypeStruct(dp.shape, BF16), jax.ShapeDtypeStruct((s_len, HGW), F32),
                             jax.ShapeDtypeStruct((1, HGW), F32)],
                  aliases={5: 0}, carry=carry)(o0, o1, p, hw4, dy, dp)


def _rope_tables(s_len):
    t = np.arange(s_len)
    inv = ROPE_THETA ** (-np.arange(0, 32, 2, dtype=np.float64) / 32)
    def half(pos):
        ang = pos[:, None].astype(np.float64) * inv[None, :]
        return (np.concatenate([np.cos(ang), np.cos(ang)], 1), np.concatenate([-np.sin(ang), np.sin(ang)], 1))
    cr, sr = half(t // GRID_W)
    cc, sc = half(t % GRID_W)
    cos = np.concatenate([cr, cc, cr, cc], 1)
    sin = np.concatenate([sr, sc, sr, sc], 1)
    cos = np.concatenate([np.ones((L, 128)), cos], 0)
    sin = np.concatenate([np.zeros((L, 128)), sin], 0)
    return jnp.asarray(cos, F32), jnp.asarray(sin, F32)


def _blockdiag(n, w):
    i = np.arange(n)
    return jnp.asarray((i[:, None] // w == i[None, :] // w) / float(w), F32)


def _dup_matrix():
    m = np.zeros((128, 512), np.float32)
    for g in range(2):
        for j in range(4):
            for dd in range(HDIM):
                m[64 * g + dd, 256 * g + 64 * j + dd] = 1.0
    return m


def _head_mean(x, blockdiag):
    return _dot(x, blockdiag, prec=lax.Precision.HIGH)


def _rot(x):
    n = x.shape[1]
    lane = lax.broadcasted_iota(jnp.int32, x.shape, 1)
    return jnp.where((lane % 32) < 16, pltpu.roll(x, n - 16, 1), pltpu.roll(x, 16, 1))


def _qk_prep(p, cos, sin, qnw8, knw2, bd512, bd128, dup):
    tt = p.shape[0]

    def body(q_ref, kv_ref, cos_ref, sin_ref, qw_ref, kw_ref, b5_ref, b1_ref, dup_ref,
             qr_ref, k4_ref, v4_ref):
        cos_, sin_ = cos_ref[...], sin_ref[...]
        q = q_ref[...]
        qn = q * lax.rsqrt(_head_mean(q * q, b5_ref[...]) + EPS) * qw_ref[...]
        cos4 = jnp.concatenate([cos_] * 4, axis=1)
        sin4 = jnp.concatenate([sin_] * 4, axis=1)
        qr_ref[...] = ((qn * cos4 + _rot(qn) * sin4) * (HDIM ** -0.5)).astype(BF16)
        kv = kv_ref[...]
        k, v = kv[:, :128], kv[:, 128:]
        kn = k * lax.rsqrt(_head_mean(k * k, b1_ref[...]) + EPS) * kw_ref[...]
        kr = kn * cos_ + _rot(kn) * sin_
        k4_ref[...] = _bdot(kr, dup_ref[...]).astype(BF16)
        v4_ref[...] = _bdot(v, dup_ref[...]).astype(BF16)

    row = lambda w, cb: pl.BlockSpec((TM, w), lambda i: (i, cb))
    out = jax.ShapeDtypeStruct((tt, ATW), BF16)
    return _pcall(body, name="qk_prep", grid=(tt // TM,),
                  in_specs=[row(ATW, C_QRAW), row(256, C_KV), row(128, 0), row(128, 0),
                            _full((1, ATW)), _full((1, 128)), _full((ATW, ATW)), _full((128, 128)),
                            _full((128, ATW))],
                  out_specs=[row(ATW, 0)] * 3, out_shape=[out] * 3)(
                      p, p, cos, sin, qnw8, knw2, bd512, bd128, dup)


def _attn_masks(i, nb):
    r = lax.broadcasted_iota(jnp.int32, (4 * BLK, 3 * BLK + L), 0) % BLK
    c = lax.broadcasted_iota(jnp.int32, (4 * BLK, 3 * BLK + L), 1)
    kpos = (i - 1) * BLK + c
    loc = (jnp.abs(c - BLK - r) <= BLK) & (kpos >= 0) & (kpos < nb * BLK)
    return loc | (c >= 3 * BLK)


def _stack_mask():
    r = lax.broadcasted_iota(jnp.int32, (4 * BLK, 256), 0)
    lane = lax.broadcasted_iota(jnp.int32, (4 * BLK, 256), 1)
    return (r // BLK) == (lane // HDIM)


def _stack_heads(xg, fill=0.0):
    x4 = jnp.concatenate([xg] * 4, axis=0)
    return jnp.where(_stack_mask(), x4, jnp.full_like(x4, fill))


def _unstack_heads(x4):
    out = jnp.where(_lane_mask(0), x4[0:BLK], 0.0)
    for j in range(1, 4):
        out = out + jnp.where(_lane_mask(j), x4[j * BLK:(j + 1) * BLK], 0.0)
    return out


def _per_head_rows(vals):
    return jnp.concatenate([jnp.broadcast_to(v, (BLK, 1)) for v in vals], axis=0)


def _lane_mask(j):
    lane = lax.broadcasted_iota(jnp.int32, (1, 256), 1)
    return (lane // HDIM) == j


def _attn_specs(nb):
    blk = lambda off: pl.BlockSpec((BLK, ATW), lambda i: (jnp.clip(i + off, 0, nb - 1) + 2, 0))
    ctx = pl.BlockSpec((L, ATW), lambda i: (0, 0))
    return blk, ctx


def _attn_fwd(qr, k4, v4, sinks, carry=None):
    tt = qr.shape[0]
    s_len = tt - L
    nb = s_len // BLK

    def body(sk_ref, q_ref, kp, ko, kn, kc, vp, vo, vn, vc, y_ref, lse_ref):
        i = pl.program_id(0)
        valid = _attn_masks(i, nb)
        q = q_ref[...]
        ys, lses = [], []
        for g in range(2):
            gs = slice(256 * g, 256 * g + 256)
            kcat = jnp.concatenate([kp[:, gs], ko[:, gs], kn[:, gs], kc[:, gs]], axis=0)
            vcat = jnp.concatenate([vp[:, gs], vo[:, gs], vn[:, gs], vc[:, gs]], axis=0)
            sink4 = _per_head_rows([sk_ref[4 * g + j] for j in range(4)])
            q4 = _stack_heads(q[:, gs])
            o_parts, l_parts = [], []
            for hp in range(2):
                rows = slice(2 * BLK * hp, 2 * BLK * (hp + 1))
                sink = sink4[rows]
                s = jnp.where(valid[rows], _dot(q4[rows], kcat, NT), -1e30)
                m = jnp.maximum(jnp.max(s, axis=-1, keepdims=True), sink)
                e = jnp.exp(s - m)
                den = jnp.sum(e, axis=-1, keepdims=True) + jnp.exp(sink - m)
                o_parts.append(_bdot(e * (1.0 / den), vcat))
                l_parts.append(jnp.broadcast_to(m + jnp.log(den), (2 * BLK, 256)))
            ys.append(_unstack_heads(jnp.concatenate(o_parts, axis=0)))
            lses.append(_unstack_heads(jnp.concatenate(l_parts, axis=0)))
        y_ref[...] = jnp.concatenate(ys, axis=1).astype(BF16)
        lse_ref[...] = jnp.concatenate(lses, axis=1)

    blk, ctx = _attn_specs(nb)
    out = pl.BlockSpec((BLK, ATW), lambda i: (i, 0))
    return _pcall(body, name="attn_fwd", grid=(nb,),
                  in_specs=[pl.BlockSpec(memory_space=pltpu.SMEM), blk(0),
                            blk(-1), blk(0), blk(1), ctx, blk(-1), blk(0), blk(1), ctx],
                  out_specs=[out, out],
                  out_shape=[jax.ShapeDtypeStruct((s_len, ATW), BF16),
                             jax.ShapeDtypeStruct((s_len, ATW), F32)], carry=carry)(
                      sinks, qr, k4, k4, k4, k4, v4, v4, v4, v4)


def _attn_bwd(qr, k4, v4, sinks, y, lse, dy, carry=None):
    tt = qr.shape[0]
    s_len = tt - L
    nb = s_len // BLK

    def body(sk_ref, q_ref, kp, ko, kn, kc, vp, vo, vn, vc, y_ref, lse_ref, dy_ref,
             dq_ref, dkw_ref, dvw_ref, dkc_ref, dvc_ref, dsk_ref):
        i = pl.program_id(0)

        @pl.when(i == 0)
        def _():
            dkc_ref[...] = jnp.zeros_like(dkc_ref)
            dvc_ref[...] = jnp.zeros_like(dvc_ref)
            dsk_ref[...] = jnp.zeros_like(dsk_ref)

        valid = _attn_masks(i, nb)
        q = q_ref[...]
        dy_ = dy_ref[...]
        dly = dy_ * y_ref[...].astype(F32)
        lse_ = lse_ref[...]
        dqs = []
        for g in range(2):
            gs = slice(256 * g, 256 * g + 256)
            kcat = jnp.concatenate([kp[:, gs], ko[:, gs], kn[:, gs], kc[:, gs]], axis=0)
            vcat = jnp.concatenate([vp[:, gs], vo[:, gs], vn[:, gs], vc[:, gs]], axis=0)
            q4 = _stack_heads(q[:, gs])
            dy4 = _stack_heads(dy_[:, gs]).astype(BF16)
            lse4 = jnp.max(_stack_heads(lse_[:, gs], fill=-1e30), axis=-1, keepdims=True)
            delta = jnp.sum(_stack_heads(dly[:, gs]), axis=-1, keepdims=True)
            sink = _per_head_rows([sk_ref[4 * g + j] for j in range(4)])
            pr = jnp.where(valid, jnp.exp(_dot(q4, kcat, NT) - lse4), 0.0)
            dsb = (pr * (_dot(dy4, vcat, NT) - delta)).astype(BF16)
            dsink = jnp.exp(sink - lse4) * delta
            for j in range(4):
                dsk_ref[4 * g + j:4 * g + j + 1, :] += jnp.broadcast_to(
                    -jnp.sum(dsink[j * BLK:(j + 1) * BLK], axis=0, keepdims=True), (1, 128))
            dqs.append(_unstack_heads(_dot(dsb, kcat)))
            dkg = _dot(dsb, q4, TN)
            dvg = _dot(pr.astype(BF16), dy4, TN)
            dkw_ref[0, :, gs] = dkg[:3 * BLK]
            dvw_ref[0, :, gs] = dvg[:3 * BLK]
            dkc_ref[:, gs] += dkg[3 * BLK:]
            dvc_ref[:, gs] += dvg[3 * BLK:]
        dq_ref[...] = jnp.concatenate(dqs, axis=1)

    blk, ctx = _attn_specs(nb)
    out = pl.BlockSpec((BLK, ATW), lambda i: (i, 0))
    win = pl.BlockSpec((1, 3 * BLK, ATW), lambda i: (i, 0, 0))
    acc = _full((L, ATW))
    return _pcall(body, name="attn_bwd", grid=(nb,),
                  in_specs=[pl.BlockSpec(memory_space=pltpu.SMEM), blk(0),
                            blk(-1), blk(0), blk(1), ctx, blk(-1), blk(0), blk(1), ctx, out, out, out],
                  out_specs=[out, win, win, acc, acc, _full((8, 128))],
                  out_shape=[jax.ShapeDtypeStruct((s_len, ATW), F32),
                             jax.ShapeDtypeStruct((nb, 3 * BLK, ATW), F32),
                             jax.ShapeDtypeStruct((nb, 3 * BLK, ATW), F32),
                             jax.ShapeDtypeStruct((L, ATW), F32), jax.ShapeDtypeStruct((L, ATW), F32),
                             jax.ShapeDtypeStruct((8, 128), F32)], carry=carry)(
                      sinks, qr, k4, k4, k4, k4, v4, v4, v4, v4, y, lse, dy)


def _attn_post(p, cos, sin, qnw8, knw2, bd512, bd128, dupt, dq, dkw, dvw, dkc, dvc, dp, carry=None):
    tt = p.shape[0]
    s_len = tt - L
    nb = s_len // BLK
    nctx = L // BLK

    def body(q_ref, kv_ref, cos_ref, sin_ref, qw_ref, kw_ref, b5_ref, b1_ref, dupt_ref,
             dq_ref, kwp, kwo, kwn, vwp, vwo, vwn, dkc_ref, dvc_ref, _dp_in,
             dp_ref, dqw_ref, dkw_ref):
        t = pl.program_id(0)
        j = t - nctx

        @pl.when(t == 0)
        def _():
            dqw_ref[...] = jnp.zeros_like(dqw_ref)
            dkw_ref[...] = jnp.zeros_like(dkw_ref)

        is_lat = t >= nctx
        cos_, sin_ = cos_ref[...], sin_ref[...]
        has_p = is_lat & (j >= 1)
        has_n = is_lat & (j <= nb - 2)
        dk4 = (jnp.where(is_lat, kwo[0], dkc_ref[...]) + jnp.where(has_p, kwp[0], 0.0)
               + jnp.where(has_n, kwn[0], 0.0))
        dv4 = (jnp.where(is_lat, vwo[0], dvc_ref[...]) + jnp.where(has_p, vwp[0], 0.0)
               + jnp.where(has_n, vwn[0], 0.0))
        dkr = _dot(dk4, dupt_ref[...], prec=HI)
        dv = _dot(dv4, dupt_ref[...], prec=HI)
        kv = kv_ref[...]
        k = kv[:, :128]
        kw = kw_ref[...]
        rk = lax.rsqrt(_head_mean(k * k, b1_ref[...]) + EPS)
        xk = k * rk
        dkn = dkr * cos_ + _rot(dkr * sin_)
        dxk = dkn * kw
        dk = rk * (dxk - xk * _head_mean(dxk * xk, b1_ref[...]))
        dkw_ref[...] += jnp.sum(dkn * xk, axis=0, keepdims=True)
        q = q_ref[...]
        qw = qw_ref[...]
        rq = lax.rsqrt(_head_mean(q * q, b5_ref[...]) + EPS)
        xq = q * rq
        cos4 = jnp.concatenate([cos_] * 4, axis=1)
        sin4 = jnp.concatenate([sin_] * 4, axis=1)
        dqr = jnp.where(is_lat, dq_ref[...], 0.0) * (HDIM ** -0.5)
        dqn = dqr * cos4 + _rot(dqr * sin4)
        dxq = dqn * qw
        dqraw = rq * (dxq - xq * _head_mean(dxq * xq, b5_ref[...]))
        dqw_ref[...] += jnp.sum(dqn * xq, axis=0, keepdims=True)
        dp_ref[...] = jnp.concatenate([dqraw, dk, dv], axis=1).astype(BF16)

    row = lambda w, cb: pl.BlockSpec((BLK, w), lambda t: (t, cb))
    lat = pl.BlockSpec((BLK, ATW), lambda t: (jnp.maximum(t - nctx, 0), 0))

    def part(off):
        return pl.BlockSpec((1, BLK, ATW), lambda t: (jnp.clip(t - nctx + off, 0, nb - 1), 1 - off, 0))

    cacc = pl.BlockSpec((BLK, ATW), lambda t: (jnp.minimum(t, nctx - 1), 0))
    return _pcall(body, name="attn_post", grid=(tt // BLK,),
                  in_specs=[row(ATW, C_QRAW), row(256, C_KV), row(128, 0), row(128, 0),
                            _full((1, ATW)), _full((1, 128)), _full((ATW, ATW)), _full((128, 128)),
                            _full((ATW, 128)), lat, part(-1), part(0), part(1), part(-1), part(0), part(1),
                            cacc, cacc, ANY],
                  out_specs=[pl.BlockSpec((BLK, 768), lambda t: (t, C_QKV)), _full((1, ATW)), _full((1, 128))],
                  out_shape=[jax.ShapeDtypeStruct(dp.shape, BF16), jax.ShapeDtypeStruct((1, ATW), F32),
                             jax.ShapeDtypeStruct((1, 128), F32)],
                  aliases={18: 0}, carry=carry)(p, p, cos, sin, qnw8, knw2, bd512, bd128, dupt,
                                   dq, dkw, dkw, dkw, dvw, dvw, dvw, dkc, dvc, dp)


def _branch_merge(y_hg, y_at, bh4, ba4, p):
    s_len = y_hg.shape[0]

    def body(yh_ref, ya_ref, bh_ref, ba_ref, gh_ref, ga_ref, ah_ref, aa_ref, m_ref):
        yh, ya = yh_ref[...], ya_ref[...]
        ah = jnp.concatenate([_bdot(yh, bh_ref[j]) for j in range(4)], axis=1)
        aa = jnp.concatenate([_bdot(ya, ba_ref[j]) for j in range(4)], axis=1)
        ah_ref[...] = ah.astype(BF16)
        aa_ref[...] = aa.astype(BF16)
        m_ref[...] = (_sig(gh_ref[...]) * ah + _sig(ga_ref[...]) * aa).astype(BF16)

    row = pl.BlockSpec((TM, D), lambda i: (i, 0))
    y = pl.BlockSpec((TM, HGW), lambda i: (i, 0))
    f = jax.ShapeDtypeStruct((s_len, D), BF16)
    return _pcall(body, name="branch_merge", grid=(s_len // TM,),
                  in_specs=[y, y, _full(bh4.shape), _full(ba4.shape),
                            pl.BlockSpec((TM, D), lambda i: (i + 1, 2)), pl.BlockSpec((TM, D), lambda i: (i + 1, 3))],
                  out_specs=[row, row, row],
                  out_shape=[f, f, jax.ShapeDtypeStruct((s_len, D), BF16)])(y_hg, y_at, bh4, ba4, p, p)


def _branch_bwd(dmh, dma, bh4, ba4, y_hg, y_at):
    s_len = dmh.shape[0]
    nk = s_len // TS
    ns = D // 4

    def body(dh_ref, da_ref, bh_ref, ba_ref, yh_ref, ya_ref, dyh_ref, dya_ref, gh_ref, ga_ref, acc_h, acc_a):
        t = pl.program_id(0)

        @pl.when(t == 0)
        def _():
            acc_h[...] = jnp.zeros_like(acc_h)
            acc_a[...] = jnp.zeros_like(acc_a)

        for d_ref, w_ref, y_ref, dy_ref, acc in ((dh_ref, bh_ref, yh_ref, dyh_ref, acc_h),
                                                 (da_ref, ba_ref, ya_ref, dya_ref, acc_a)):
            y = y_ref[...]
            dy = jnp.zeros((TS, HGW), F32)
            for j in range(4):
                dj = d_ref[:, j * ns:(j + 1) * ns]
                dy = dy + _bdot(dj, w_ref[j], NT)
                acc[j] += _bdot(y, dj, TN)
            dy_ref[...] = dy

        @pl.when(t == nk - 1)
        def _():
            gh_ref[...] = acc_h[...].astype(BF16)
            ga_ref[...] = acc_a[...].astype(BF16)

    dm = pl.BlockSpec((TS, D), lambda t: (t, 0))
    y = pl.BlockSpec((TS, HGW), lambda t: (t, 0))
    w = _full(bh4.shape)
    fy = jax.ShapeDtypeStruct((s_len, HGW), F32)
    gw = jax.ShapeDtypeStruct(bh4.shape, BF16)
    return _pcall(body, name="branch_bwd", grid=(nk,), in_specs=[dm, dm, w, w, y, y],
                  out_specs=[y, y, w, w], out_shape=[fy, fy, gw, gw],
                  scratch=[pltpu.VMEM(bh4.shape, F32)] * 2)(dmh, dma, bh4, ba4, y_hg, y_at)


def _merge_bwd(dattn, w_o, mixed, ah, aa, p, carry=None):
    tt = p.shape[0]
    s_len = tt - L
    nt = tt // TM

    def body(da_ref, wo_ref, mx_ref, ah_ref, aa_ref, gh_ref, ga_ref, dp_ref, dmh_ref, dma_ref, go_ref, acc):
        i = pl.program_id(0)

        @pl.when(i == 0)
        def _():
            dp_ref[...] = jnp.zeros_like(dp_ref)
            acc[...] = jnp.zeros_like(acc)

        @pl.when(i >= 1)
        def _():
            da = da_ref[...]
            acc[...] += _bdot(mx_ref[...], da, TN)
            dm_ = _bdot(da, wo_ref[...], NT)
            sh, sa = _sig(gh_ref[...]), _sig(ga_ref[...])
            dp_ref[...] = jnp.concatenate([dm_ * ah_ref[...].astype(F32) * sh * (1.0 - sh),
                                           dm_ * aa_ref[...].astype(F32) * sa * (1.0 - sa)], axis=1).astype(BF16)
            dmh_ref[...] = (dm_ * sh).astype(BF16)
            dma_ref[...] = (dm_ * sa).astype(BF16)

        @pl.when(i == nt - 1)
        def _():
            go_ref[...] = acc[...].astype(BF16)

    lat = pl.BlockSpec((TM, D), lambda i: (jnp.maximum(i - 1, 0), 0))
    return _pcall(body, name="merge_bwd", grid=(nt,),
                  in_specs=[lat, _full((D, D)), lat, lat, lat, pl.BlockSpec((TM, D), lambda i: (i, 2)),
                            pl.BlockSpec((TM, D), lambda i: (i, 3))],
                  out_specs=[pl.BlockSpec((TM, 2 * D), lambda i: (i, C_GATES)), lat, lat, _full((D, D))],
                  out_shape=[jax.ShapeDtypeStruct((tt, NCOL), BF16), jax.ShapeDtypeStruct((s_len, D), BF16),
                             jax.ShapeDtypeStruct((s_len, D), BF16), jax.ShapeDtypeStruct((D, D), BF16)],
                  scratch=[pltpu.VMEM((D, D), F32)], carry=carry)(dattn, w_o, mixed, ah, aa, p, p)


def _local_step(x, ctx, tgt, mod, modc, nw1, nw2, lg, hw, qnw, knw, sinks,
                w_in, wts, dist=None):
    s_len = x.shape[0]
    tt = s_len + L
    ss1 = jnp.stack([modc, mod[0:2]])
    ss2 = mod[3:5][None]
    g1, g2 = mod[2:3], mod[5:6]
    hw4 = jnp.tile(hw, (1, 4))
    qnw8 = jnp.tile(qnw, (1, 8))
    knw2 = jnp.tile(knw, (1, 2))
    cos, sin = _rope_tables(s_len)
    bd512, bd128 = _blockdiag(ATW, HDIM), _blockdiag(128, HDIM)
    dupm = _dup_matrix()
    dup, dupt = jnp.asarray(dupm, BF16), jnp.asarray(dupm.T, F32)
    tmt = tt

    def four(b):
        return b.reshape(4, 2 * b.shape[1], b.shape[2])

    def halves(g):
        return g.reshape(4, 2, g.shape[1] // 2, g.shape[2])

    h = _mod1(ctx, x, nw1, ss1)
    if dist is None:
        bh4, ba4, w_o, g4, u4, dn4 = wts
        p = _mm_in(h, w_in, tmt)
        o0, st0 = _hgrn_fwd(p, lg, rev=False)
        o1, st1, y_hg = _hgrn_fwd(p, lg, rev=True, readout=(o0, hw4))
    else:
        core, chip = dist
        half = wts[3].shape[1] // 2
        p, (o8, g8a) = _mm_in(h, w_in, tmt, carry=_carry_gather([wts[2], wts[3]], rows=[None, (0, half)]))
        (o0, st0), (g8,) = _hgrn_fwd(p, lg, rev=False, carry=_carry_gather([g8a], rows=[(half, half)]))
        (o1, st1, y_hg), (dn8a, bh8, ba8) = _hgrn_fwd(
            p, lg, rev=True, readout=(o0, hw4),
            carry=_carry_gather([wts[5], wts[0], wts[1]], rows=[(0, half), None, None]))
        bh4, ba4, w_o, g4 = four(bh8), four(ba8), four(o8).reshape(D, D), four(g8)
    qr, k4, v4 = _qk_prep(p, cos, sin, qnw8, knw2, bd512, bd128, dup)
    if dist is None:
        y_at, lse = _attn_fwd(qr, k4, v4, sinks)
    else:
        (y_at, lse), (u8, dn8) = _attn_fwd(qr, k4, v4, sinks,
                                           carry=_carry_gather([wts[4], dn8a], rows=[None, (half, half)]))
        u4, dn4 = four(u8), four(dn8)
    ah, aa, mixed = _branch_merge(y_hg, y_at, bh4, ba4, p)
    ao, x1, h2 = _out_proj_mod2(mixed, w_o, x, g1, nw2, ss2)
    a4, b4, z4 = _ffn_up(h2, g4, u4)
    sq, dx2, dyb, dg2 = _ffn_down_loss(z4, dn4, x1, g2, tgt)

    da4, db4 = _ffn_dz(dyb, dn4, a4, b4)
    g_dn = _ffn_gdn(z4, dyb)
    if dist is None:
        dh2 = _ffn_dh2(da4, db4, g4, u4)
    else:
        dn_units = [halves(g_dn)]
        dh2, dn_recv = _ffn_dh2(da4, db4, g4, u4, carry=_carry_pairx(dn_units))
        dn_pairs = _rs_pair_add(dn_units, dn_recv, core)
    if dist is None:
        g_g, g_u = _ffn_ggu(h2, da4, db4)
    else:
        (g_g, g_u), c_dn = _ffn_ggu(h2, da4, db4, carry=_carry_chipx(dn_pairs))
        red_dn = _rs_chip_add(dn_pairs, c_dn, core, chip)
    dx1, dattn, dss2, dnw2, dg1 = _mod2_bwd(x1, dh2, dx2, ao, nw2, ss2, g1)
    if dist is None:
        dp, dmh, dma, g_o = _merge_bwd(dattn, w_o, mixed, ah, aa, p)
    else:
        gu_units = [halves(g_g), halves(g_u)]
        (dp, dmh, dma, g_o), gu_recv = _merge_bwd(dattn, w_o, mixed, ah, aa, p, carry=_carry_pairx(gu_units))
        ffn_pairs = list(dn_pairs) + list(_rs_pair_add(gu_units, gu_recv, core))
    dy_hg, dy_at, g_bh, g_ba = _branch_bwd(dmh, dma, bh4, ba4, y_hg, y_at)
    if dist is None:
        dp, do, dhw4 = _readout_bwd(o0, o1, p, hw4, dy_hg, dp)
        dq, dkw, dvw, dkc, dvc, dsk = _attn_bwd(qr, k4, v4, sinks, y_at, lse, dy_at)
        dp, dqnw8, dknw2 = _attn_post(p, cos, sin, qnw8, knw2, bd512, bd128, dupt, dq, dkw, dvw, dkc, dvc, dp)
    else:
        mix_units = [halves(g_bh), halves(g_ba), halves(g_o.reshape(4, D // 4, D))]
        (dp, do, dhw4), mix_recv = _readout_bwd(o0, o1, p, hw4, dy_hg, dp, carry=_carry_pairx(mix_units))
        mix_pairs = _rs_pair_add(mix_units, mix_recv, core)
        (dq, dkw, dvw, dkc, dvc, dsk), bwd = _attn_bwd(
            qr, k4, v4, sinks, y_at, lse, dy_at,
            carry=_carry_join(_carry_chipx(ffn_pairs[1:2]), _carry_sibx(red_dn)))
        red_g = _rs_chip_add(ffn_pairs[1:2], bwd[0:1], core, chip)
        (dp, dqnw8, dknw2), post = _attn_post(
            p, cos, sin, qnw8, knw2, bd512, bd128, dupt, dq, dkw, dvw, dkc, dvc, dp, carry=_carry_sibx(red_g))
    if dist is None:
        dp, dv0, dq0, dlg0 = _hgrn_bwd(p, lg, do, st0, dp, None, rev=False)
        dp, dlg1 = _hgrn_bwd(p, lg, do, st1, dp, (dv0, dq0), rev=True)
    else:
        (dp, dv0, dq0, dlg0), c_u = _hgrn_bwd(p, lg, do, st0, dp, None, rev=False,
                                              carry=_carry_chipx(ffn_pairs[2:3]))
        red_u = _rs_chip_add(ffn_pairs[2:3], c_u, core, chip)
        (dp, dlg1), last = _hgrn_bwd(p, lg, do, st1, dp, (dv0, dq0), rev=True,
                                     carry=_carry_join(_carry_chipx(mix_pairs), _carry_sibx(red_u)))
        mix_reds = _rs_chip_add(mix_pairs, last[0:3], core, chip)
        ffn_done = bwd[1:2] + post[0:1] + last[3:4]
    g_in = _mm_gin(dp, h, tmt)
    if dist is None:
        dh = _mm_dh(dp, w_in, tmt)
        gx, dss1, dnw1 = _mod1_bwd(ctx, x, dh, dx1, nw1, ss1)
        rs = None
    else:
        in_units = [halves(g_in.reshape(4, NCOL // 4, D))]
        in_pairs = _rs_pair_add(in_units, _pair_exchange(in_units), core)
        first_rows, rest_rows = _split_rows(in_pairs[0].shape[1], IN_ROWS_WITH_DH)
        small_rows, rest_rows = _split_rows(rest_rows[1], IN_ROWS_WITH_SMALL, first=rest_rows[0])
        dh, both = _mm_dh(dp, w_in, tmt, carry=_carry_join(_carry_chipx(in_pairs, rows=first_rows),
                                                           _carry_sibx(mix_reds)))
        in_part, mix_done = both[0:1], both[1:4]
        gx, dss1, dnw1 = _mod1_bwd(ctx, x, dh, dx1, nw1, ss1)
        rs = dict(ffn_done=ffn_done, mix_done=mix_done, in_pairs=in_pairs, in_part=in_part, small_rows=small_rows,
                  rest_rows=rest_rows)

    dmod = jnp.concatenate([dss1[1], dg1, dss2, dg2], axis=0)
    dmodc = dss1[0]
    raw = (dss1, dg1, dss2, dg2, dnw1, dnw2, dhw4, dqnw8, dknw2, dsk, dlg0, dlg1)
    small = dict(raw=raw, dmod=dmod, dmodc=dmodc, dnw1=dnw1, dnw2=dnw2,
                 dhw=dhw4.reshape(4, HGD).sum(0, keepdims=True),
                 dqnw=dqnw8.reshape(8, HDIM).sum(0, keepdims=True),
                 dknw=dknw2.reshape(2, HDIM).sum(0, keepdims=True),
                 dsinks=dsk[:, 0], dlg=jnp.concatenate([dlg0, dlg1], axis=0))
    big = dict(w_in=g_in, w_bh=g_bh, w_ba=g_ba, w_o=g_o, w_g=g_g, w_u=g_u, w_dn=g_dn)
    return sq, gx, big, small, rs


def _place():
    x, y, c = lax.axis_index("x"), lax.axis_index("y"), lax.axis_index("c")
    return x, y, c


def _gather_blocks(x_refs, out_refs, send_sems, recv_sems, local_sems):
    n = len(out_refs)
    x, y, c = _place()
    me, sibling = (x, y, c), (x, y, 1 - c)
    chips = [(1 - x, y), (x, 1 - y), (1 - x, 1 - y)]

    def slot(u, px, py, pc):
        return out_refs[u].at[4 * px + 2 * py + pc]

    def copy(u, k, block, to, src=None):
        return pltpu.make_async_remote_copy(
            src_ref=slot(u, *block) if src is None else src, dst_ref=slot(u, *block),
            send_sem=send_sems.at[u, k], recv_sem=recv_sems.at[u, k], device_id=to, device_id_type=MESH)

    mines = [pltpu.make_async_copy(x_refs[u], slot(u, *me), local_sems.at[u]) for u in range(n)]
    for cp in mines:
        cp.start()
    first = []
    for u in range(n):
        first.append(copy(u, 0, me, sibling, src=x_refs[u]))
        first += [copy(u, 1 + j, me, (*chip, c), src=x_refs[u]) for j, chip in enumerate(chips)]
    for cp in first:
        cp.start()
    passed = []
    for j, chip in enumerate(chips):
        for u in range(n):
            copy(u, 1 + j, (*chip, c), me).wait_recv()
            fwd = copy(u, 4 + j, (*chip, c), sibling)
            fwd.start()
            passed.append(fwd)
    for u in range(n):
        copy(u, 0, sibling, me).wait_recv()
    for j, chip in enumerate(chips):
        for u in range(n):
            copy(u, 4 + j, (*chip, 1 - c), me).wait_recv()
    for cp in first + passed:
        cp.wait_send()
    for cp in mines:
        cp.wait()


def _gather_sems(n):
    return [pltpu.SemaphoreType.DMA((n, 7)), pltpu.SemaphoreType.DMA((n, 7)), pltpu.SemaphoreType.DMA((n,))]


def _cast_place(ws, c, dev):
    n = len(ws)

    def body(s_ref, *refs):
        for u in range(n):
            refs[n + u][0] = refs[u][...].astype(BF16)

    in_specs, out_specs, out_shape = [], [], []
    for w in ws:
        q, cols = w.shape[0] // 4, w.shape[1]
        in_specs.append(pl.BlockSpec((q, cols), lambda i, s: (2 * s[0] + i, 0)))
        out_specs.append(pl.BlockSpec((1, q, cols), lambda i, s: (s[1], i, 0)))
        out_shape.append(jax.ShapeDtypeStruct((8, 2 * q, cols), BF16))
    return pl.pallas_call(
        body, name="cast_place",
        grid_spec=pltpu.PrefetchScalarGridSpec(num_scalar_prefetch=1, grid=(2,), in_specs=in_specs,
                                               out_specs=out_specs),
        out_shape=_out_hbm(out_shape),
        compiler_params=pltpu.CompilerParams(vmem_limit_bytes=48 << 20))(jnp.stack([c, dev]), *_in_hbm(ws))


def _gather_phases(out_refs, send_sems, recv_sems, rows=None):
    n = len(out_refs)
    x, y, c = _place()
    me, sibling = (x, y, c), (x, y, 1 - c)
    chips = [(1 - x, y), (x, 1 - y), (1 - x, 1 - y)]

    def copy(u, k, block, to):
        px, py, pc = block
        ref = out_refs[u].at[4 * px + 2 * py + pc]
        if rows is not None and rows[u] is not None:
            ref = ref.at[pl.ds(rows[u][0], rows[u][1])]
        return pltpu.make_async_remote_copy(src_ref=ref, dst_ref=ref, send_sem=send_sems.at[u, k],
                                            recv_sem=recv_sems.at[u, k], device_id=to, device_id_type=MESH)

    def start():
        for u in range(n):
            copy(u, 0, me, sibling).start()
            for j, chip in enumerate(chips):
                copy(u, 1 + j, me, (*chip, c)).start()

    def mid():
        for j, chip in enumerate(chips):
            for u in range(n):
                copy(u, 1 + j, (*chip, c), me).wait_recv()
                copy(u, 4 + j, (*chip, c), sibling).start()

    def end():
        for u in range(n):
            copy(u, 0, sibling, me).wait_recv()
        for j, chip in enumerate(chips):
            for u in range(n):
                copy(u, 4 + j, (*chip, 1 - c), me).wait_recv()
        for u in range(n):
            copy(u, 0, me, sibling).wait_send()
            for j, chip in enumerate(chips):
                copy(u, 1 + j, me, (*chip, c)).wait_send()
                copy(u, 4 + j, (*chip, c), sibling).wait_send()

    return start, mid, end


def _carry_gather(bufs, rows=None):
    n = len(bufs)
    return _Carry(bufs, [jax.ShapeDtypeStruct(b.shape, b.dtype) for b in bufs], {u: u for u in range(n)},
                  [pltpu.SemaphoreType.DMA((n, 7)), pltpu.SemaphoreType.DMA((n, 7))],
                  lambda ins, outs, sems: _gather_phases(outs, *sems, rows=rows), "both")


def _ag_small(raw, sq, pairs, rows, into):
    def body(dss1, dg1, dss2, dg2, dnw1, dnw2, dhw4, dqnw8, dknw2, dsk, dlg0, dlg1, sq_ref, p_ref, _into,
             out_ref, tot_ref, r_ref, blk, send_sems, recv_sems, chip_send, chip_recv):
        _peer_barrier("both")
        x, y, c = _place()
        start, mid, end = _gather_phases([out_ref], send_sems, recv_sems)
        chip_start, _, chip_end = _chipx_phases([p_ref], [r_ref], chip_send, chip_recv, rows=rows)
        blk[...] = jnp.zeros_like(blk)
        blk[0:2, :] = dss1[1]
        blk[2:3, :] = dg1[...]
        blk[3:5, :] = dss2[...]
        blk[5:6, :] = dg2[...]
        blk[6:8, :] = dss1[0]
        blk[8:9, :] = dnw1[...]
        blk[9:10, :] = dnw2[...]
        blk[10:11, 0:HGW] = dhw4[...]
        blk[10:11, HGW:D] = dqnw8[...]
        blk[11:12, 0:128] = dknw2[...]
        blk[12:14, 0:HGW] = dlg0[0]
        blk[14:16, 0:HGW] = dlg1[0]
        blk[16:24, 0:128] = dsk[...]
        blk[24:25, :] = sq_ref[...]
        out_ref[4 * x + 2 * y + c] = blk[...]
        start()
        chip_start()
        mid()
        end()
        acc = out_ref[0]
        for i in range(1, 8):
            acc = acc + out_ref[i]
        tot_ref[...] = acc
        chip_end()

    vm = pl.BlockSpec(memory_space=pltpu.VMEM)
    g2, tot, part = pl.pallas_call(
        body, name="ag_small",
        out_shape=[jax.ShapeDtypeStruct((8, 32, D), F32), jax.ShapeDtypeStruct((32, D), F32),
                   jax.ShapeDtypeStruct(into.shape, into.dtype)],
        in_specs=[vm] * 13 + [ANY, ANY], out_specs=[vm, vm, ANY], input_output_aliases={14: 2},
        scratch_shapes=[pltpu.VMEM((32, D), F32)] + [pltpu.SemaphoreType.DMA((1, 7))] * 2
        + [pltpu.SemaphoreType.DMA((1, 3))] * 2,
        compiler_params=pltpu.CompilerParams(collective_id=BARRIER_IDS["both"]))(*raw, sq, *_in_hbm([pairs[0], into]))
    return (g2, tot), part


def _pairx_shapes(units):
    return [jax.ShapeDtypeStruct((4,) + g.shape[2:], g.dtype) for g in units]


def _pairx_phases(g_refs, r_refs, send_sems, recv_sems):
    n = len(g_refs)
    x, y, c = _place()
    cps = [pltpu.make_async_remote_copy(
        src_ref=g_refs[u].at[j, 1 - c], dst_ref=r_refs[u].at[j], send_sem=send_sems.at[u, j],
        recv_sem=recv_sems.at[u, j], device_id=(x, y, 1 - c), device_id_type=MESH)
        for u in range(n) for j in range(4)]

    def start():
        for cp in cps:
            cp.start()

    def end():
        for cp in cps:
            cp.wait()

    return start, None, end


def _carry_pairx(units):
    n = len(units)
    return _Carry(units, _pairx_shapes(units), {},
                  [pltpu.SemaphoreType.DMA((n, 4)), pltpu.SemaphoreType.DMA((n, 4))],
                  lambda ins, outs, sems: _pairx_phases(ins, outs, *sems), "sib")


def _pair_exchange(units):
    n = len(units)

    def body(*refs):
        _peer_barrier("sib")
        start, _, end = _pairx_phases(refs[:n], refs[n:2 * n], *refs[2 * n:])
        start()
        end()

    return pl.pallas_call(
        body, name="pair_exchange", out_shape=_pairx_shapes(units), in_specs=[ANY] * n, out_specs=[ANY] * n,
        scratch_shapes=[pltpu.SemaphoreType.DMA((n, 4))] * 2,
        compiler_params=pltpu.CompilerParams(collective_id=BARRIER_IDS["sib"]))(*_in_hbm(units))


IN_ROWS_WITH_DH = 0.6


IN_ROWS_WITH_SMALL = 0.45


def _split_rows(h, share, first=0):
    k = int(h * share) // BF16_SUBLANES * BF16_SUBLANES
    return (first, k), (first + k, h - k)


def _rs_pair_add(units, recvs, c):
    n = len(units)

    def body(c_ref, *refs):
        for u in range(n):
            refs[2 * n + u][...] = (refs[u][0].astype(F32) + refs[n + u][...].astype(F32)).astype(BF16)

    in_specs, out_specs, out_shape = [], [], []
    for g in units:
        h, w = g.shape[2:]
        in_specs.append(pl.BlockSpec((1, 1, h, w), lambda j, cr: (j, cr[0], 0, 0)))
    for g in units:
        h, w = g.shape[2:]
        in_specs.append(pl.BlockSpec((1, h, w), lambda j, cr: (j, 0, 0)))
        out_specs.append(pl.BlockSpec((1, h, w), lambda j, cr: (j, 0, 0)))
        out_shape.append(jax.ShapeDtypeStruct((4, h, w), BF16))
    return pl.pallas_call(
        body, name="rs_pair_add",
        grid_spec=pltpu.PrefetchScalarGridSpec(num_scalar_prefetch=1, grid=(4,), in_specs=in_specs,
                                               out_specs=out_specs),
        out_shape=_out_hbm(out_shape),
        compiler_params=pltpu.CompilerParams(vmem_limit_bytes=48 << 20))(
            c.reshape(1), *_in_hbm(list(units) + list(recvs)))


def _chipx_phases(p_refs, r_refs, send_sems, recv_sems, rows=None):
    n = len(p_refs)
    x, y, c = _place()
    k = 2 * x + y

    def part(ref):
        return ref if rows is None else ref.at[pl.ds(rows[0], rows[1])]

    sends = []
    for d in range(1, 4):
        j = (k + d) % 4
        for u in range(n):
            sends.append(pltpu.make_async_remote_copy(
                src_ref=part(p_refs[u].at[j]), dst_ref=part(r_refs[u].at[k]), send_sem=send_sems.at[u, d - 1],
                recv_sem=recv_sems.at[u, d - 1], device_id=(j // 2, j % 2, c), device_id_type=MESH))

    def start():
        for cp in sends:
            cp.start()

    def end():
        for d in range(1, 4):
            src = (k + 4 - d) % 4
            for u in range(n):
                pltpu.make_async_remote_copy(
                    src_ref=part(p_refs[u].at[src]), dst_ref=part(r_refs[u].at[src]),
                    send_sem=send_sems.at[u, d - 1], recv_sem=recv_sems.at[u, d - 1], device_id=(x, y, c),
                    device_id_type=MESH).wait_recv()
        for cp in sends:
            cp.wait_send()

    return start, None, end


def _carry_chipx(pairs, rows=None, into=None):
    n = len(pairs)
    sems = [pltpu.SemaphoreType.DMA((n, 3)), pltpu.SemaphoreType.DMA((n, 3))]
    shapes = [jax.ShapeDtypeStruct(p.shape, p.dtype) for p in pairs]
    if into is None:
        return _Carry(pairs, shapes, {}, sems, lambda ins, outs, s: _chipx_phases(ins, outs, *s, rows=rows),
                      "chips")
    return _Carry(list(pairs) + list(into), shapes, {n + u: u for u in range(n)}, sems,
                  lambda ins, outs, s: _chipx_phases(ins[:n], outs, *s, rows=rows), "chips")


def _rs_chip_add(pairs, contribs, c, chip):
    n = len(pairs)

    def body(s_ref, *refs):
        for u in range(n):
            a, b, c_, d = refs[4 * u:4 * u + 4]
            refs[4 * n + u][0] = ((a[0].astype(F32) + b[0].astype(F32)) + c_[0].astype(F32)) + d[0].astype(F32)

    in_specs, out_specs, out_shape, args = [], [], [], []
    for p, r in zip(pairs, contribs):
        h, w = p.shape[1] // 2, p.shape[2]
        in_specs += [pl.BlockSpec((1, h, w), functools.partial(lambda d, i, s: ((s[1] + d) % 4, i, 0), d))
                     for d in range(4)]
        args += [p, r, r, r]
        out_specs.append(pl.BlockSpec((1, h, w), lambda i, s: (s[0], i, 0)))
        out_shape.append(jax.ShapeDtypeStruct((2, 2 * h, w), F32))
    return pl.pallas_call(
        body, name="rs_chip_add",
        grid_spec=pltpu.PrefetchScalarGridSpec(num_scalar_prefetch=1, grid=(2,), in_specs=in_specs,
                                               out_specs=out_specs),
        out_shape=_out_hbm(out_shape),
        compiler_params=pltpu.CompilerParams(vmem_limit_bytes=48 << 20))(jnp.stack([c, chip]), *_in_hbm(args))


def _rs_sibling_gather(reds, blks):
    n, k = len(reds), len(blks)
    vm = pl.BlockSpec(memory_space=pltpu.VMEM)

    def body(*refs):
        blk_in, red_out, blk_out = refs[n:n + k], refs[n + k:2 * n + k], refs[2 * n + k:2 * (n + k)]
        sems = refs[2 * (n + k):]
        _peer_barrier("both")
        start, _, end = _sibx_phases(red_out, *sems[:2])
        start()
        _gather_blocks(blk_in, blk_out, *sems[2:])
        end()

    res = pl.pallas_call(
        body, name="rs_sibling_gather",
        out_shape=[jax.ShapeDtypeStruct(r.shape, r.dtype) for r in reds]
        + [jax.ShapeDtypeStruct((8,) + b.shape, b.dtype) for b in blks],
        in_specs=[ANY] * n + [vm] * k, out_specs=[ANY] * n + [vm] * k,
        input_output_aliases={u: u for u in range(n)},
        scratch_shapes=[pltpu.SemaphoreType.DMA((n,))] * 2 + _gather_sems(k),
        compiler_params=pltpu.CompilerParams(collective_id=BARRIER_IDS["both"]))(*_in_hbm(reds), *blks)
    return res[:n], res[n:]


def _sibx_phases(o_refs, send_sems, recv_sems):
    n = len(o_refs)
    x, y, c = _place()
    cps = [pltpu.make_async_remote_copy(
        src_ref=o_refs[u].at[c], dst_ref=o_refs[u].at[c], send_sem=send_sems.at[u], recv_sem=recv_sems.at[u],
        device_id=(x, y, 1 - c), device_id_type=MESH) for u in range(n)]

    def start():
        for cp in cps:
            cp.start()

    def end():
        for u in range(n):
            cps[u].wait_send()
            pltpu.make_async_remote_copy(
                src_ref=o_refs[u].at[1 - c], dst_ref=o_refs[u].at[1 - c], send_sem=send_sems.at[u],
                recv_sem=recv_sems.at[u], device_id=(x, y, 1 - c), device_id_type=MESH).wait_recv()

    return start, None, end


def _carry_sibx(reds):
    n = len(reds)
    return _Carry(reds, [jax.ShapeDtypeStruct(r.shape, r.dtype) for r in reds], {u: u for u in range(n)},
                  [pltpu.SemaphoreType.DMA((n,))] * 2, lambda ins, outs, sems: _sibx_phases(outs, *sems), "sib")


def _prologue(blk, c_ctx, w, b, in8):
    n = w.shape[1]

    def body(blk_ref, cctx_ref, w_ref, b_ref, _in_in, g0_ref, c16_ref, g1_ref, in_ref, s1, r1, l1, s2, r2, s3, r3):
        _peer_barrier("both")
        x, y, c = _place()
        start, mid, end = _gather_phases([in_ref], s3, r3)
        start_mod, mid_mod, end_mod = _gather_phases([g1_ref], s2, r2)
        _gather_blocks([blk_ref], [g0_ref], s1, r1, l1)
        start()
        c16 = jnp.concatenate([g0_ref[i, 0:1, :] for i in range(8)] + [cctx_ref[...], jnp.zeros((7, D), F32)],
                              axis=0)
        c16_ref[...] = c16
        g1_ref[4 * x + 2 * y + c] = _dot(c16 * _sig(c16), w_ref[...], prec=HI) + b_ref[...]
        start_mod()
        mid()
        mid_mod()
        end()
        end_mod()

    vm = pl.BlockSpec(memory_space=pltpu.VMEM)
    return pl.pallas_call(
        body, name="prologue",
        out_shape=[jax.ShapeDtypeStruct((8, 8, D), F32), jax.ShapeDtypeStruct((16, D), F32),
                   jax.ShapeDtypeStruct((8, 16, n), F32), jax.ShapeDtypeStruct(in8.shape, in8.dtype)],
        in_specs=[vm, vm, vm, vm, ANY], out_specs=[vm, vm, vm, ANY], input_output_aliases={4: 3},
        scratch_shapes=_gather_sems(1) + [pltpu.SemaphoreType.DMA((1, 7))] * 4,
        compiler_params=pltpu.CompilerParams(vmem_limit_bytes=48 << 20,
                                             collective_id=BARRIER_IDS["both"]))(blk, c_ctx, w, b, in8)


def _ada_bwd(c16, dmod16, w, carry=None):
    n = w.shape[1]
    tn = 512

    def body(c_ref, d_ref, w_ref, gw_ref, gc_ref):
        j = pl.program_id(0)

        @pl.when(j == 0)
        def _():
            gc_ref[...] = jnp.zeros_like(gc_ref)

        cc = c_ref[...]
        dm = d_ref[...]
        gw_ref[...] = _dot(cc * _sig(cc), dm, TN, prec=HI)
        gc_ref[...] += _dot(dm, w_ref[...], NT, prec=HI)

    return _pcall(body, name="ada_bwd", grid=(n // tn,),
                  in_specs=[_full((16, D)), pl.BlockSpec((16, tn), lambda j: (0, j)),
                            pl.BlockSpec((D, tn), lambda j: (0, j))],
                  out_specs=[pl.BlockSpec((D, tn), lambda j: (0, j)), _full((16, D))],
                  out_shape=[jax.ShapeDtypeStruct((D, n), F32),
                             jax.ShapeDtypeStruct((16, D), F32)], carry=carry)(c16, dmod16, w)


def _adam_math(w, g, m, v):
    c1 = 1.0 - ADAM_B1 ** ADAM_STEP
    c2 = 1.0 - ADAM_B2 ** ADAM_STEP
    nm = ADAM_B1 * m + (1.0 - ADAM_B1) * g
    nv = ADAM_B2 * v + (1.0 - ADAM_B2) * (g * g)
    return -ADAM_LR * ((nm / c1) / (jnp.sqrt(nv / c2) + ADAM_EPS) + ADAM_WD * w), nm, nv


def _adamw_small(ws, gs, ms, vs):
    n = len(ws)

    def body(*refs):
        for u in range(n):
            d_, nm, nv = _adam_math(refs[u][...], refs[n + u][...], refs[2 * n + u][...], refs[3 * n + u][...])
            refs[4 * n + u][...] = d_
            refs[5 * n + u][...] = nm
            refs[6 * n + u][...] = nv

    specs = [_full(w.shape) for w in ws]
    shapes = [jax.ShapeDtypeStruct(w.shape, F32) for w in ws]
    out = _pcall(body, name="adamw_small", grid=(1,), in_specs=specs * 4, out_specs=specs * 3,
                 out_shape=shapes * 3)(*ws, *gs, *ms, *vs)
    return out[:n], out[n:2 * n], out[2 * n:]


def _cctx_grad(parts, c_ctx):
    def body(p_ref, c_ref, o_ref):
        acc = p_ref[0:1, :]
        for k in range(1, 4):
            acc = acc + p_ref[k:k + 1, :]
        cc = c_ref[...]
        s = _sig(cc)
        o_ref[...] = acc * (s * (1.0 + cc * (1.0 - s)))

    return _pcall(body, name="cctx_grad", grid=(1,), in_specs=[_full(parts.shape), _full((1, D))],
                  out_specs=_full((1, D)), out_shape=jax.ShapeDtypeStruct((1, D), F32))(parts, c_ctx)


ADAM_STEPS = 8


def _adamw_multi(ws, gs, ms, vs, *, name):
    n = len(ws)

    def body(*refs):
        for u in range(n):
            g = refs[n + u][...]
            refs[4 * n + u][...] = g
            refs[5 * n + u][...], refs[6 * n + u][...], refs[7 * n + u][...] = _adam_math(
                refs[u][...], g, refs[2 * n + u][...], refs[3 * n + u][...])

    specs = [pl.BlockSpec((w.shape[0] // ADAM_STEPS, w.shape[1]), lambda i: (i, 0)) for w in ws]
    shapes = [jax.ShapeDtypeStruct(w.shape, F32) for w in ws]
    out = _pcall(body, name=name, grid=(ADAM_STEPS,), in_specs=specs * 4, out_specs=specs * 4,
                 out_shape=shapes * 4)(*ws, *gs, *ms, *vs)
    return out[:n], out[n:2 * n], out[2 * n:3 * n], out[3 * n:]


def kernel(x, c, ctx, c_ctx, w_ada, b_ada, norm_mix_w, norm_ffn_w, w_in, hgrn_lb_logits, hgrn_norm_w, q_norm_w, k_norm_w, attn_sinks, w_branch_hgrn, w_branch_attn, w_out, w_ffn_gate, w_ffn_up, w_ffn_down, loss_target, m_c_ctx, m_w_ada, m_b_ada, m_norm_mix_w, m_norm_ffn_w, m_w_in, m_hgrn_lb_logits, m_hgrn_norm_w, m_q_norm_w, m_k_norm_w, m_attn_sinks, m_w_branch_hgrn, m_w_branch_attn, m_w_out, m_w_ffn_gate, m_w_ffn_up, m_w_ffn_down, v_c_ctx, v_w_ada, v_b_ada, v_norm_mix_w, v_norm_ffn_w, v_w_in, v_hgrn_lb_logits, v_hgrn_norm_w, v_q_norm_w, v_k_norm_w, v_attn_sinks, v_w_branch_hgrn, v_w_branch_attn, v_w_out, v_w_ffn_gate, v_w_ffn_up, v_w_ffn_down):
    xi, yi, ci = _place()
    chip = 2 * xi + yi
    dev = 2 * chip + ci
    s_len = x.shape[1]

    shards = [w_in[0].T, w_branch_hgrn[0], w_branch_attn[0], w_out[0], w_ffn_gate[0].T, w_ffn_up[0].T,
              w_ffn_down[0]]
    bufs = _cast_place(shards, ci, dev)

    lbrow = jnp.pad(hgrn_lb_logits.reshape(1, 512), ((0, 0), (0, D - 512)))
    blk = jnp.concatenate([c, lbrow, jnp.zeros((6, D), F32)], axis=0)
    nada = w_ada.shape[2]
    b_sh = lax.dynamic_slice(b_ada, (0, chip * nada), (1, nada))
    g0, c16, g1, in8 = _prologue(blk, c_ctx[None], w_ada[0], b_sh, bufs[0])
    lg = g0[0::2, 1, :512].reshape(4, 2, 2, 128).transpose(1, 2, 0, 3).reshape(2, 2, HGW)
    modall = g1[0::2].transpose(1, 0, 2).reshape(16, 4 * nada)
    mod = lax.dynamic_slice(modall, (dev, 0), (1, 6 * D)).reshape(6, D)
    modc = modall[8].reshape(6, D)[:2]

    sq, gx, _, small, rs = _local_step(
        x[0], ctx[0], loss_target[0], mod, modc, norm_mix_w, norm_ffn_w, lg, hgrn_norm_w, q_norm_w,
        k_norm_w, attn_sinks[0], in8.reshape(NCOL, D), bufs[1:], dist=(ci, chip))

    def whole(r):
        return r.reshape(2 * r.shape[1], r.shape[2])

    g_dn, g_g, g_u = [whole(r) for r in rs["ffn_done"]]
    g_bh, g_ba, g_o = [whole(r) for r in rs["mix_done"]]
    in_pairs = rs["in_pairs"]

    (g2, tot), in_part = _ag_small(small["raw"], sq, in_pairs, rs["small_rows"], rs["in_part"][0])
    loss = 0.5 * jnp.sum(tot[24]) / D
    dmodc_tot = jnp.pad(tot[6:8].reshape(1, 2 * D), ((0, 0), (0, 4 * D)))
    g_b_ada = tot[0:6].reshape(1, 6 * D) + dmodc_tot
    dmod16 = jnp.concatenate([g2[:, 0:6].reshape(8, 6 * D), dmodc_tot, jnp.zeros((7, 6 * D), F32)], axis=0)
    (g_w_ada, gc_part), in_contribs = _ada_bwd(
        c16, lax.dynamic_slice(dmod16, (0, chip * nada), (16, nada)), w_ada[0],
        carry=_carry_chipx(in_pairs, rows=rs["rest_rows"], into=[in_part]))
    in_reds, (g3,) = _rs_sibling_gather(_rs_chip_add(in_pairs, in_contribs, ci, chip), [gc_part[8:16]])
    g_in = whole(in_reds[0])
    g_c_ctx = _cctx_grad(g3[0::2, 0], c_ctx[None])[0]
    g_nw1 = tot[8:9]
    g_nw2 = tot[9:10]
    g_hw = tot[10, :HGW].reshape(4, HGD).sum(0, keepdims=True)
    g_qnw = tot[10, HGW:].reshape(8, HDIM).sum(0, keepdims=True)
    g_knw = tot[11, :128].reshape(2, HDIM).sum(0, keepdims=True)
    g_sinks = tot[16:24, 0][None]
    g_lg = lax.dynamic_slice(tot[12:16, :HGW].reshape(2, 2, HGW), (0, 0, chip * 128), (2, 2, 128))

    names = ["c_ctx", "w_ada", "b_ada", "norm_mix_w", "norm_ffn_w", "w_in", "hgrn_lb_logits", "hgrn_norm_w",
             "q_norm_w", "k_norm_w", "attn_sinks", "w_branch_hgrn", "w_branch_attn", "w_out", "w_ffn_gate",
             "w_ffn_up", "w_ffn_down"]
    ws = dict(zip(names, [c_ctx, w_ada, b_ada, norm_mix_w, norm_ffn_w, w_in, hgrn_lb_logits, hgrn_norm_w,
                          q_norm_w, k_norm_w, attn_sinks, w_branch_hgrn, w_branch_attn, w_out, w_ffn_gate,
                          w_ffn_up, w_ffn_down]))
    ms = dict(zip(names, [m_c_ctx, m_w_ada, m_b_ada, m_norm_mix_w, m_norm_ffn_w, m_w_in, m_hgrn_lb_logits,
                          m_hgrn_norm_w, m_q_norm_w, m_k_norm_w, m_attn_sinks, m_w_branch_hgrn,
                          m_w_branch_attn, m_w_out, m_w_ffn_gate, m_w_ffn_up, m_w_ffn_down]))
    vs = dict(zip(names, [v_c_ctx, v_w_ada, v_b_ada, v_norm_mix_w, v_norm_ffn_w, v_w_in, v_hgrn_lb_logits,
                          v_hgrn_norm_w, v_q_norm_w, v_k_norm_w, v_attn_sinks, v_w_branch_hgrn,
                          v_w_branch_attn, v_w_out, v_w_ffn_gate, v_w_ffn_up, v_w_ffn_down]))
    transposed = ("w_in", "w_ffn_gate", "w_ffn_up")

    def view(a, n):
        return a[0].T if n in transposed else a[0]

    def unview(a, n):
        return a.T[None] if n in transposed else a[None]

    delta, new_m, new_v, grads = {}, {}, {}, {}

    def big_adamw(group, gs, name):
        g_, d_, m_, v_ = _adamw_multi([view(ws[n], n) for n in group], gs, [view(ms[n], n) for n in group],
                                      [view(vs[n], n) for n in group], name=name)
        for i, n in enumerate(group):
            grads[n], delta[n], new_m[n], new_v[n] = (unview(g_[i], n), unview(d_[i], n), unview(m_[i], n),
                                                      unview(v_[i], n))

    big_adamw(["w_ffn_down", "w_ffn_gate", "w_ffn_up", "w_out", "w_branch_hgrn", "w_branch_attn"],
              [g_dn, g_g, g_u, g_o, g_bh, g_ba], "adamw_first")
    big_adamw(["w_in", "w_ada"], [g_in, g_w_ada], "adamw_second")
    grads.update(c_ctx=g_c_ctx, b_ada=g_b_ada, norm_mix_w=g_nw1, norm_ffn_w=g_nw2, hgrn_lb_logits=g_lg,
                 hgrn_norm_w=g_hw, q_norm_w=g_qnw, k_norm_w=g_knw, attn_sinks=g_sinks)
    small_names = [n for n in names if n not in delta]

    def two_d(a):
        return a.reshape(1, -1) if a.ndim == 1 else a

    sd, sm_, sv = _adamw_small(*[[two_d(d[n]) for n in small_names] for d in (ws, grads, ms, vs)])
    for i, n in enumerate(small_names):
        for dst, src in ((delta, sd), (new_m, sm_), (new_v, sv)):
            dst[n] = src[i].reshape(ws[n].shape)
    return (loss, gx[None], *[grads[n] for n in names], *[delta[n] for n in names],
            *[new_m[n] for n in names], *[new_v[n] for n in names])
```
